```python
import math
import jax, jax.numpy as jnp
from jax import lax
import numpy as np

D_MODEL = 1024
BATCH = 8
SEQ = 2048
DEPTH = 2

N_HEADS_A = 4
HEAD_DIM_A = 128
WIDTH_A = N_HEADS_A * HEAD_DIM_A
CONV_A = 4
CHUNK = 64
SSM_WIDTH = 512
SSM_GROUP = 16
N_GROUPS = SSM_WIDTH // SSM_GROUP
SSM_STATE = 64
D_IN_AB = 4 * WIDTH_A + 2 * N_HEADS_A + SSM_WIDTH
D_MIX_AB = WIDTH_A + SSM_WIDTH
POOL_WINDOWS = (2, 4, 8, 16)
POOL_GROUP = D_MODEL // len(POOL_WINDOWS)
MEM_LEN = 256
N_HEADS_X = 4
HEAD_DIM_X = D_MODEL // N_HEADS_X
D_FF = 2816
CONV_FFN = 3
RMS_EPS = 1e-6
N_EVEN = (DEPTH + 1) // 2
N_ODD = DEPTH // 2

kernel_name = "hybrid_deltanet_s5_pool_decoder"


def rms_norm(x, g):
    xf = x.astype(jnp.float32)
    y = xf * lax.rsqrt(jnp.mean(xf * xf, axis=-1, keepdims=True) + RMS_EPS)
    return (y * g.astype(jnp.float32)).astype(x.dtype)


def causal_dwconv(x, w):
    k_w = w.shape[0]
    t = x.shape[1]
    xp = jnp.pad(x, ((0, 0), (k_w - 1, 0), (0, 0)))
    return sum(xp[:, i:i + t, :] * w[i] for i in range(k_w))


def l2_normalize(x):
    return x * lax.rsqrt(jnp.sum(x * x, axis=-1, keepdims=True) + 1e-6)


def gated_delta_rule_chunked(q, k, v, g, beta):
    b, h, t, dk = q.shape
    dv = v.shape[-1]
    n = t // CHUNK
    q = q * (dk ** -0.5)
    q, k, v = (a.reshape(b, h, n, CHUNK, a.shape[-1]) for a in (q, k, v))
    g = jnp.cumsum(g.reshape(b, h, n, CHUNK), axis=-1)
    beta = beta.reshape(b, h, n, CHUNK)
    causal = jnp.tril(jnp.ones((CHUNK, CHUNK), bool))
    strict = jnp.tril(jnp.ones((CHUNK, CHUNK), bool), -1)
    diff = g[..., :, None] - g[..., None, :]
    decay = jnp.where(causal, jnp.exp(jnp.where(causal, diff, 0.0)), 0.0)
    k_beta = k * beta[..., None]
    a_mat = jnp.where(strict, jnp.einsum('bhncd,bhnsd->bhncs', k_beta, k) * decay, 0.0)
    eye = jnp.eye(CHUNK, dtype=jnp.float32)
    rhs = jnp.concatenate([v * beta[..., None], k_beta * jnp.exp(g)[..., None]], axis=-1)
    sol = lax.linalg.triangular_solve(a_mat + eye, rhs, left_side=True, lower=True,
                                      unit_diagonal=True)
    u, w = sol[..., :dv], sol[..., dv:]
    qk = jnp.where(causal, jnp.einsum('bhncd,bhnsd->bhncs', q, k) * decay, 0.0)

    def step(state, xs):
        q_c, k_c, u_c, w_c, qk_c, g_c = xs
        v_new = u_c - jnp.einsum('bhck,bhkv->bhcv', w_c, state)
        o = (jnp.einsum('bhck,bhkv->bhcv', q_c * jnp.exp(g_c)[..., None], state)
             + jnp.einsum('bhcs,bhsv->bhcv', qk_c, v_new))
        g_last = g_c[..., -1:]
        state = (state * jnp.exp(g_last)[..., None]
                 + jnp.einsum('bhck,bhcv->bhkv', k_c * jnp.exp(g_last - g_c)[..., None], v_new))
        return state, o

    xs = tuple(jnp.moveaxis(a, 2, 0) for a in (q, k, u, w, qk, g))
    s0 = jnp.zeros((b, h, dk, dv), jnp.float32)
    _, o = lax.scan(step, s0, xs)
    return jnp.moveaxis(o, 0, 2).reshape(b, h, t, dv)


def s5_ssm(u, lam_re, lam_im, b_re, b_im, c_re, c_im, d, log_dt):
    uf = u.astype(jnp.float32)
    dt = jnp.exp(log_dt.astype(jnp.float32))[:, None]
    lr, li = lam_re.astype(jnp.float32), lam_im.astype(jnp.float32)
    mag = jnp.exp(lr * dt)
    ang = li * dt
    lb_re, lb_im = mag * jnp.cos(ang), mag * jnp.sin(ang)
    den = lr * lr + li * li
    nr, ni = lb_re - 1.0, lb_im
    coef_re = (nr * lr + ni * li) / den
    coef_im = (ni * lr - nr * li) / den
    br, bi = b_re.astype(jnp.float32), b_im.astype(jnp.float32)
    bb_re = coef_re[..., None] * br - coef_im[..., None] * bi
    bb_im = coef_re[..., None] * bi + coef_im[..., None] * br
    bu_re = jnp.einsum('gph,btgh->btgp', bb_re, uf)
    bu_im = jnp.einsum('gph,btgh->btgp', bb_im, uf)
    a_re = jnp.broadcast_to(lb_re, bu_re.shape)
    a_im = jnp.broadcast_to(lb_im, bu_re.shape)

    def combine(e1, e2):
        a1r, a1i, b1r, b1i = e1
        a2r, a2i, b2r, b2i = e2
        return (a1r * a2r - a1i * a2i,
                a1r * a2i + a1i * a2r,
                a2r * b1r - a2i * b1i + b2r,
                a2r * b1i + a2i * b1r + b2i)

    _, _, xr, xi = lax.associative_scan(combine, (a_re, a_im, bu_re, bu_im), axis=1)
    y = (jnp.einsum('ghp,btgp->btgh', c_re.astype(jnp.float32), xr)
         - jnp.einsum('ghp,btgp->btgh', c_im.astype(jnp.float32), xi)
         + d.astype(jnp.float32) * uf)
    return y


def hybrid_delta_ssm_mixer(xn, w_in, conv_qkv, a_log, dt_bias, onorm_g,
                           lam_re, lam_im, b_re, b_im, c_re, c_im, ssm_d, log_dt,
                           w_glu, b_glu, w_out):
    b, t, _ = xn.shape
    h = xn @ w_in
    qkv, gate, beta_logit, alpha_logit, u = jnp.split(
        h, [3 * WIDTH_A, 4 * WIDTH_A, 4 * WIDTH_A + N_HEADS_A, 4 * WIDTH_A + 2 * N_HEADS_A],
        axis=-1)
    qkv = jax.nn.silu(causal_dwconv(qkv, conv_qkv)).astype(jnp.float32)
    q, k, v = (a.reshape(b, t, N_HEADS_A, HEAD_DIM_A).transpose(0, 2, 1, 3)
               for a in jnp.split(qkv, 3, axis=-1))
    q, k = l2_normalize(q), l2_normalize(k)
    beta = jax.nn.sigmoid(beta_logit.astype(jnp.float32)).transpose(0, 2, 1)
    g = (-jnp.exp(a_log.astype(jnp.float32))
         * jax.nn.softplus(alpha_logit.astype(jnp.float32) + dt_bias.astype(jnp.float32))
         ).transpose(0, 2, 1)
    o = gated_delta_rule_chunked(q, k, v, g, beta).transpose(0, 2, 1, 3)
    o = rms_norm(o, onorm_g) * jax.nn.silu(
        gate.reshape(b, t, N_HEADS_A, HEAD_DIM_A).astype(jnp.float32))
    y_a = o.reshape(b, t, WIDTH_A)
    y = s5_ssm(u.reshape(b, t, N_GROUPS, SSM_GROUP), lam_re, lam_im, b_re, b_im,
               c_re, c_im, ssm_d, log_dt)
    y = jax.nn.gelu(y.reshape(b, t, SSM_WIDTH))
    y_b = y * jax.nn.sigmoid(y @ w_glu.astype(jnp.float32) + b_glu.astype(jnp.float32))
    mixed = jnp.concatenate([y_a, y_b], axis=-1).astype(xn.dtype)
    return mixed @ w_out


def multiscale_pool_mixer(xn, pool_w, pool_scale):
    b, t, d = xn.shape
    xf = xn.astype(jnp.float32)
    cs = jnp.pad(lax.cumsum(xf, axis=1), ((0, 0), (1, 0), (0, 0)))
    pos_count = jnp.arange(1, t + 1, dtype=jnp.float32)[:, None]
    outs = []
    for gi, win in enumerate(POOL_WINDOWS):
        sl = slice(gi * POOL_GROUP, (gi + 1) * POOL_GROUP)
        csg = cs[..., sl]
        lower = jnp.pad(csg, ((0, 0), (win, 0), (0, 0)))[:, 1:t + 1]
        mean = (csg[:, 1:] - lower) / jnp.minimum(pos_count, float(win))
        outs.append(jnp.einsum('btc,ce->bte', mean - xf[..., sl],
                               pool_w[gi].astype(jnp.float32)))
    return (jnp.concatenate(outs, axis=-1) * pool_scale.astype(jnp.float32)).astype(xn.dtype)


def mem_cross_attention(xn, mem_n, wq, wkv, wo):
    b, t, _ = xn.shape
    m = mem_n.shape[1]
    q = (xn @ wq).reshape(b, t, N_HEADS_X, HEAD_DIM_X)
    kv = (mem_n @ wkv).reshape(b, m, 2, N_HEADS_X, HEAD_DIM_X)
    k, v = kv[:, :, 0], kv[:, :, 1]
    s = jnp.einsum('bthd,bmhd->bhtm', q, k).astype(jnp.float32) * (HEAD_DIM_X ** -0.5)
    p = jax.nn.softmax(s, axis=-1).astype(v.dtype)
    o = jnp.einsum('bhtm,bmhd->bthd', p, v).reshape(b, t, N_HEADS_X * HEAD_DIM_X)
    return o @ wo


def conv_ffn(xn, w_up, conv_w, w_down):
    h = causal_dwconv(xn @ w_up, conv_w)
    gate, val = jnp.split(h, 2, axis=-1)
    return (jax.nn.silu(gate) * val) @ w_down


def _fwd_setup_inputs(seed: int = 0) -> dict:
    key = jax.random.key(seed)
    ks = iter(jax.random.split(key, 48))

    def nrm(shape, scale):
        return jax.random.normal(next(ks), shape, jnp.float32) * scale

    def gain(shape):
        return 1.0 + nrm(shape, 0.05)

    E, O, L, D = N_EVEN, N_ODD, DEPTH, D_MODEL
    x = nrm((BATCH, SEQ, D), 1.0)
    mem = nrm((BATCH, MEM_LEN, D), 1.0)
    norm_mix_g = gain((L, D))
    norm_xa_g = gain((L, D))
    norm_ffn_g = gain((L, D))
    norm_mem_g = gain((D,))
    norm_final_g = gain((D,))
    w_in_ab = nrm((E, D, D_IN_AB), D ** -0.5)
    conv_qkv_a = nrm((E, CONV_A, 3 * WIDTH_A), CONV_A ** -0.5)
    a_log_a = jnp.log(jax.random.uniform(next(ks), (E, N_HEADS_A), jnp.float32, 1.0, 16.0))
    dt0 = jnp.exp(jax.random.uniform(next(ks), (E, N_HEADS_A), jnp.float32,
                                     math.log(1e-3), math.log(1e-1)))
    dt_bias_a = dt0 + jnp.log(-jnp.expm1(-dt0))
    onorm_g_a = gain((E, HEAD_DIM_A))
    ssm_lambda_re = -0.5 + nrm((E, N_GROUPS, SSM_STATE), 0.01)
    ssm_lambda_im = jnp.broadcast_to(
        math.pi * jnp.arange(SSM_STATE, dtype=jnp.float32), (E, N_GROUPS, SSM_STATE)
    ) + nrm((E, N_GROUPS, SSM_STATE), 0.001)
    ssm_b_re = nrm((E, N_GROUPS, SSM_STATE, SSM_GROUP), (2 * SSM_GROUP) ** -0.5)
    ssm_b_im = nrm((E, N_GROUPS, SSM_STATE, SSM_GROUP), (2 * SSM_GROUP) ** -0.5)
    ssm_c_re = nrm((E, N_GROUPS, SSM_GROUP, SSM_STATE), (2 * SSM_STATE) ** -0.5)
    ssm_c_im = nrm((E, N_GROUPS, SSM_GROUP, SSM_STATE), (2 * SSM_STATE) ** -0.5)
    ssm_d = nrm((E, N_GROUPS, SSM_GROUP), 1.0)
    ssm_log_dt = jax.random.uniform(next(ks), (E, N_GROUPS), jnp.float32,
                                    math.log(1e-3), math.log(1e-1))
    w_glu_b = nrm((E, SSM_WIDTH, SSM_WIDTH), SSM_WIDTH ** -0.5)
    b_glu_b = nrm((E, SSM_WIDTH), 0.01)
    w_out_ab = nrm((E, D_MIX_AB, D), D_MIX_AB ** -0.5)
    pool_w = nrm((O, len(POOL_WINDOWS), POOL_GROUP, POOL_GROUP), POOL_GROUP ** -0.5)
    pool_scale = 1.0 + nrm((O, D), 0.1)
    xa_wq = nrm((L, D, D), D ** -0.5)
    xa_wkv = nrm((L, D, 2 * D), D ** -0.5)
    xa_wo = nrm((L, D, D), D ** -0.5)
    ffn_w_up = nrm((L, D, 2 * D_FF), D ** -0.5)
    ffn_conv = nrm((L, CONV_FFN, 2 * D_FF), 0.3) + jnp.array([0.0, 0.0, 1.0], jnp.float32)[None, :, None]
    ffn_w_down = nrm((L, D_FF, D), D_FF ** -0.5)
    return {
        "x": x, "mem": mem,
        "norm_mix_g": norm_mix_g, "norm_xa_g": norm_xa_g, "norm_ffn_g": norm_ffn_g,
        "norm_mem_g": norm_mem_g, "norm_final_g": norm_final_g,
        "w_in_ab": w_in_ab, "conv_qkv_a": conv_qkv_a, "a_log_a": a_log_a,
        "dt_bias_a": dt_bias_a, "onorm_g_a": onorm_g_a,
        "ssm_lambda_re": ssm_lambda_re, "ssm_lambda_im": ssm_lambda_im,
        "ssm_b_re": ssm_b_re, "ssm_b_im": ssm_b_im, "ssm_c_re": ssm_c_re, "ssm_c_im": ssm_c_im,
        "ssm_d": ssm_d, "ssm_log_dt": ssm_log_dt, "w_glu_b": w_glu_b, "b_glu_b": b_glu_b,
        "w_out_ab": w_out_ab,
        "pool_w": pool_w, "pool_scale": pool_scale,
        "xa_wq": xa_wq, "xa_wkv": xa_wkv, "xa_wo": xa_wo,
        "ffn_w_up": ffn_w_up, "ffn_conv": ffn_conv, "ffn_w_down": ffn_w_down,
    }


def _fwd_reference(x, mem, norm_mix_g, norm_xa_g, norm_ffn_g, norm_mem_g, norm_final_g,
              w_in_ab, conv_qkv_a, a_log_a, dt_bias_a, onorm_g_a,
              ssm_lambda_re, ssm_lambda_im, ssm_b_re, ssm_b_im, ssm_c_re, ssm_c_im,
              ssm_d, ssm_log_dt, w_glu_b, b_glu_b, w_out_ab,
              pool_w, pool_scale, xa_wq, xa_wkv, xa_wo,
              ffn_w_up, ffn_conv, ffn_w_down):
    mem_n = rms_norm(mem, norm_mem_g)
    for layer in range(DEPTH):
        xn = rms_norm(x, norm_mix_g[layer])
        if layer % 2 == 0:
            e = layer // 2
            mix = hybrid_delta_ssm_mixer(
                xn, w_in_ab[e], conv_qkv_a[e], a_log_a[e], dt_bias_a[e], onorm_g_a[e],
                ssm_lambda_re[e], ssm_lambda_im[e], ssm_b_re[e], ssm_b_im[e],
                ssm_c_re[e], ssm_c_im[e], ssm_d[e], ssm_log_dt[e],
                w_glu_b[e], b_glu_b[e], w_out_ab[e])
        else:
            o = layer // 2
            mix = multiscale_pool_mixer(xn, pool_w[o], pool_scale[o])
        x = x + mix.astype(x.dtype)
        x = x + mem_cross_attention(rms_norm(x, norm_xa_g[layer]), mem_n,
                                    xa_wq[layer], xa_wkv[layer], xa_wo[layer]).astype(x.dtype)
        x = x + conv_ffn(rms_norm(x, norm_ffn_g[layer]),
                         ffn_w_up[layer], ffn_conv[layer], ffn_w_down[layer]).astype(x.dtype)
    return rms_norm(x, norm_final_g)


import jax as _jax
import jax.numpy as _jnp

TWIN_FORMAT = 'train_step'
FWD_PARAMS = ['x', 'mem', 'norm_mix_g', 'norm_xa_g', 'norm_ffn_g', 'norm_mem_g', 'norm_final_g', 'w_in_ab', 'conv_qkv_a', 'a_log_a', 'dt_bias_a', 'onorm_g_a', 'ssm_lambda_re', 'ssm_lambda_im', 'ssm_b_re', 'ssm_b_im', 'ssm_c_re', 'ssm_c_im', 'ssm_d', 'ssm_log_dt', 'w_glu_b', 'b_glu_b', 'w_out_ab', 'pool_w', 'pool_scale', 'xa_wq', 'xa_wkv', 'xa_wo', 'ffn_w_up', 'ffn_conv', 'ffn_w_down']
TWIN_WEIGHTS = ['norm_mix_g', 'norm_xa_g', 'norm_ffn_g', 'norm_mem_g', 'norm_final_g', 'w_in_ab', 'conv_qkv_a', 'a_log_a', 'dt_bias_a', 'onorm_g_a', 'ssm_lambda_re', 'ssm_lambda_im', 'ssm_b_re', 'ssm_b_im', 'ssm_c_re', 'ssm_c_im', 'ssm_d', 'ssm_log_dt', 'w_glu_b', 'b_glu_b', 'w_out_ab', 'pool_w', 'pool_scale', 'xa_wq', 'xa_wkv', 'xa_wo', 'ffn_w_up', 'ffn_conv', 'ffn_w_down']
TWIN_DIFF_INPUT = 'x'
TWIN_INPUTS = ['x', 'mem', 'norm_mix_g', 'norm_xa_g', 'norm_ffn_g', 'norm_mem_g', 'norm_final_g', 'w_in_ab', 'conv_qkv_a', 'a_log_a', 'dt_bias_a', 'onorm_g_a', 'ssm_lambda_re', 'ssm_lambda_im', 'ssm_b_re', 'ssm_b_im', 'ssm_c_re', 'ssm_c_im', 'ssm_d', 'ssm_log_dt', 'w_glu_b', 'b_glu_b', 'w_out_ab', 'pool_w', 'pool_scale', 'xa_wq', 'xa_wkv', 'xa_wo', 'ffn_w_up', 'ffn_conv', 'ffn_w_down', 'loss_target', 'm_norm_mix_g', 'm_norm_xa_g', 'm_norm_ffn_g', 'm_norm_mem_g', 'm_norm_final_g', 'm_w_in_ab', 'm_conv_qkv_a', 'm_a_log_a', 'm_dt_bias_a', 'm_onorm_g_a', 'm_ssm_lambda_re', 'm_ssm_lambda_im', 'm_ssm_b_re', 'm_ssm_b_im', 'm_ssm_c_re', 'm_ssm_c_im', 'm_ssm_d', 'm_ssm_log_dt', 'm_w_glu_b', 'm_b_glu_b', 'm_w_out_ab', 'm_pool_w', 'm_pool_scale', 'm_xa_wq', 'm_xa_wkv', 'm_xa_wo', 'm_ffn_w_up', 'm_ffn_conv', 'm_ffn_w_down', 'v_norm_mix_g', 'v_norm_xa_g', 'v_norm_ffn_g', 'v_norm_mem_g', 'v_norm_final_g', 'v_w_in_ab', 'v_conv_qkv_a', 'v_a_log_a', 'v_dt_bias_a', 'v_onorm_g_a', 'v_ssm_lambda_re', 'v_ssm_lambda_im', 'v_ssm_b_re', 'v_ssm_b_im', 'v_ssm_c_re', 'v_ssm_c_im', 'v_ssm_d', 'v_ssm_log_dt', 'v_w_glu_b', 'v_b_glu_b', 'v_w_out_ab', 'v_pool_w', 'v_pool_scale', 'v_xa_wq', 'v_xa_wkv', 'v_xa_wo', 'v_ffn_w_up', 'v_ffn_conv', 'v_ffn_w_down']
TWIN_OUTPUTS = ['loss', 'grad_x', 'grad_norm_mix_g', 'grad_norm_xa_g', 'grad_norm_ffn_g', 'grad_norm_mem_g', 'grad_norm_final_g', 'grad_w_in_ab', 'grad_conv_qkv_a', 'grad_a_log_a', 'grad_dt_bias_a', 'grad_onorm_g_a', 'grad_ssm_lambda_re', 'grad_ssm_lambda_im', 'grad_ssm_b_re', 'grad_ssm_b_im', 'grad_ssm_c_re', 'grad_ssm_c_im', 'grad_ssm_d', 'grad_ssm_log_dt', 'grad_w_glu_b', 'grad_b_glu_b', 'grad_w_out_ab', 'grad_pool_w', 'grad_pool_scale', 'grad_xa_wq', 'grad_xa_wkv', 'grad_xa_wo', 'grad_ffn_w_up', 'grad_ffn_conv', 'grad_ffn_w_down', 'delta_norm_mix_g', 'delta_norm_xa_g', 'delta_norm_ffn_g', 'delta_norm_mem_g', 'delta_norm_final_g', 'delta_w_in_ab', 'delta_conv_qkv_a', 'delta_a_log_a', 'delta_dt_bias_a', 'delta_onorm_g_a', 'delta_ssm_lambda_re', 'delta_ssm_lambda_im', 'delta_ssm_b_re', 'delta_ssm_b_im', 'delta_ssm_c_re', 'delta_ssm_c_im', 'delta_ssm_d', 'delta_ssm_log_dt', 'delta_w_glu_b', 'delta_b_glu_b', 'delta_w_out_ab', 'delta_pool_w', 'delta_pool_scale', 'delta_xa_wq', 'delta_xa_wkv', 'delta_xa_wo', 'delta_ffn_w_up', 'delta_ffn_conv', 'delta_ffn_w_down', 'new_m_norm_mix_g', 'new_m_norm_xa_g', 'new_m_norm_ffn_g', 'new_m_norm_mem_g', 'new_m_norm_final_g', 'new_m_w_in_ab', 'new_m_conv_qkv_a', 'new_m_a_log_a', 'new_m_dt_bias_a', 'new_m_onorm_g_a', 'new_m_ssm_lambda_re', 'new_m_ssm_lambda_im', 'new_m_ssm_b_re', 'new_m_ssm_b_im', 'new_m_ssm_c_re', 'new_m_ssm_c_im', 'new_m_ssm_d', 'new_m_ssm_log_dt', 'new_m_w_glu_b', 'new_m_b_glu_b', 'new_m_w_out_ab', 'new_m_pool_w', 'new_m_pool_scale', 'new_m_xa_wq', 'new_m_xa_wkv', 'new_m_xa_wo', 'new_m_ffn_w_up', 'new_m_ffn_conv', 'new_m_ffn_w_down', 'new_v_norm_mix_g', 'new_v_norm_xa_g', 'new_v_norm_ffn_g', 'new_v_norm_mem_g', 'new_v_norm_final_g', 'new_v_w_in_ab', 'new_v_conv_qkv_a', 'new_v_a_log_a', 'new_v_dt_bias_a', 'new_v_onorm_g_a', 'new_v_ssm_lambda_re', 'new_v_ssm_lambda_im', 'new_v_ssm_b_re', 'new_v_ssm_b_im', 'new_v_ssm_c_re', 'new_v_ssm_c_im', 'new_v_ssm_d', 'new_v_ssm_log_dt', 'new_v_w_glu_b', 'new_v_b_glu_b', 'new_v_w_out_ab', 'new_v_pool_w', 'new_v_pool_scale', 'new_v_xa_wq', 'new_v_xa_wkv', 'new_v_xa_wo', 'new_v_ffn_w_up', 'new_v_ffn_conv', 'new_v_ffn_w_down']
TWIN_LEAF_KINDS = {'loss': 'loss', 'grad_x': 'grad_x', 'grad_norm_mix_g': 'grad_w', 'grad_norm_xa_g': 'grad_w', 'grad_norm_ffn_g': 'grad_w', 'grad_norm_mem_g': 'grad_w', 'grad_norm_final_g': 'grad_w', 'grad_w_in_ab': 'grad_w', 'grad_conv_qkv_a': 'grad_w', 'grad_a_log_a': 'grad_w', 'grad_dt_bias_a': 'grad_w', 'grad_onorm_g_a': 'grad_w', 'grad_ssm_lambda_re': 'grad_w', 'grad_ssm_lambda_im': 'grad_w', 'grad_ssm_b_re': 'grad_w', 'grad_ssm_b_im': 'grad_w', 'grad_ssm_c_re': 'grad_w', 'grad_ssm_c_im': 'grad_w', 'grad_ssm_d': 'grad_w', 'grad_ssm_log_dt': 'grad_w', 'grad_w_glu_b': 'grad_w', 'grad_b_glu_b': 'grad_w', 'grad_w_out_ab': 'grad_w', 'grad_pool_w': 'grad_w', 'grad_pool_scale': 'grad_w', 'grad_xa_wq': 'grad_w', 'grad_xa_wkv': 'grad_w', 'grad_xa_wo': 'grad_w', 'grad_ffn_w_up': 'grad_w', 'grad_ffn_conv': 'grad_w', 'grad_ffn_w_down': 'grad_w', 'delta_norm_mix_g': 'delta_w', 'delta_norm_xa_g': 'delta_w', 'delta_norm_ffn_g': 'delta_w', 'delta_norm_mem_g': 'delta_w', 'delta_norm_final_g': 'delta_w', 'delta_w_in_ab': 'delta_w', 'delta_conv_qkv_a': 'delta_w', 'delta_a_log_a': 'delta_w', 'delta_dt_bias_a': 'delta_w', 'delta_onorm_g_a': 'delta_w', 'delta_ssm_lambda_re': 'delta_w', 'delta_ssm_lambda_im': 'delta_w', 'delta_ssm_b_re': 'delta_w', 'delta_ssm_b_im': 'delta_w', 'delta_ssm_c_re': 'delta_w', 'delta_ssm_c_im': 'delta_w', 'delta_ssm_d': 'delta_w', 'delta_ssm_log_dt': 'delta_w', 'delta_w_glu_b': 'delta_w', 'delta_b_glu_b': 'delta_w', 'delta_w_out_ab': 'delta_w', 'delta_pool_w': 'delta_w', 'delta_pool_scale': 'delta_w', 'delta_xa_wq': 'delta_w', 'delta_xa_wkv': 'delta_w', 'delta_xa_wo': 'delta_w', 'delta_ffn_w_up': 'delta_w', 'delta_ffn_conv': 'delta_w', 'delta_ffn_w_down': 'delta_w', 'new_m_norm_mix_g': 'new_m', 'new_m_norm_xa_g': 'new_m', 'new_m_norm_ffn_g': 'new_m', 'new_m_norm_mem_g': 'new_m', 'new_m_norm_final_g': 'new_m', 'new_m_w_in_ab': 'new_m', 'new_m_conv_qkv_a': 'new_m', 'new_m_a_log_a': 'new_m', 'new_m_dt_bias_a': 'new_m', 'new_m_onorm_g_a': 'new_m', 'new_m_ssm_lambda_re': 'new_m', 'new_m_ssm_lambda_im': 'new_m', 'new_m_ssm_b_re': 'new_m', 'new_m_ssm_b_im': 'new_m', 'new_m_ssm_c_re': 'new_m', 'new_m_ssm_c_im': 'new_m', 'new_m_ssm_d': 'new_m', 'new_m_ssm_log_dt': 'new_m', 'new_m_w_glu_b': 'new_m', 'new_m_b_glu_b': 'new_m', 'new_m_w_out_ab': 'new_m', 'new_m_pool_w': 'new_m', 'new_m_pool_scale': 'new_m', 'new_m_xa_wq': 'new_m', 'new_m_xa_wkv': 'new_m', 'new_m_xa_wo': 'new_m', 'new_m_ffn_w_up': 'new_m', 'new_m_ffn_conv': 'new_m', 'new_m_ffn_w_down': 'new_m', 'new_v_norm_mix_g': 'new_v', 'new_v_norm_xa_g': 'new_v', 'new_v_norm_ffn_g': 'new_v', 'new_v_norm_mem_g': 'new_v', 'new_v_norm_final_g': 'new_v', 'new_v_w_in_ab': 'new_v', 'new_v_conv_qkv_a': 'new_v', 'new_v_a_log_a': 'new_v', 'new_v_dt_bias_a': 'new_v', 'new_v_onorm_g_a': 'new_v', 'new_v_ssm_lambda_re': 'new_v', 'new_v_ssm_lambda_im': 'new_v', 'new_v_ssm_b_re': 'new_v', 'new_v_ssm_b_im': 'new_v', 'new_v_ssm_c_re': 'new_v', 'new_v_ssm_c_im': 'new_v', 'new_v_ssm_d': 'new_v', 'new_v_ssm_log_dt': 'new_v', 'new_v_w_glu_b': 'new_v', 'new_v_b_glu_b': 'new_v', 'new_v_w_out_ab': 'new_v', 'new_v_pool_w': 'new_v', 'new_v_pool_scale': 'new_v', 'new_v_xa_wq': 'new_v', 'new_v_xa_wkv': 'new_v', 'new_v_xa_wo': 'new_v', 'new_v_ffn_w_up': 'new_v', 'new_v_ffn_conv': 'new_v', 'new_v_ffn_w_down': 'new_v'}


def _forward(args):
    return _fwd_reference(*[args[k] for k in FWD_PARAMS])


def _output_shape():
    out = _jax.eval_shape(lambda: _forward(_fwd_setup_inputs(0)))
    return out.shape, out.dtype

N_MICROBATCH = 1
ADAM_LR = 0.001
ADAM_B1 = 0.9
ADAM_B2 = 0.999
ADAM_EPS = 1e-08
ADAM_WD = 0.01
ADAM_STEP = 10
PER_EXAMPLE_BATCH_AXIS = {'x': 0, 'mem': 0, 'loss_target': 0}
SHARED_INPUTS = []
_WEIGHT_DTYPES = {'norm_mix_g': _jnp.float32, 'norm_xa_g': _jnp.float32, 'norm_ffn_g': _jnp.float32, 'norm_mem_g': _jnp.float32, 'norm_final_g': _jnp.float32, 'w_in_ab': _jnp.float32, 'conv_qkv_a': _jnp.float32, 'a_log_a': _jnp.float32, 'dt_bias_a': _jnp.float32, 'onorm_g_a': _jnp.float32, 'ssm_lambda_re': _jnp.float32, 'ssm_lambda_im': _jnp.float32, 'ssm_b_re': _jnp.float32, 'ssm_b_im': _jnp.float32, 'ssm_c_re': _jnp.float32, 'ssm_c_im': _jnp.float32, 'ssm_d': _jnp.float32, 'ssm_log_dt': _jnp.float32, 'w_glu_b': _jnp.float32, 'b_glu_b': _jnp.float32, 'w_out_ab': _jnp.float32, 'pool_w': _jnp.float32, 'pool_scale': _jnp.float32, 'xa_wq': _jnp.float32, 'xa_wkv': _jnp.float32, 'xa_wo': _jnp.float32, 'ffn_w_up': _jnp.float32, 'ffn_conv': _jnp.float32, 'ffn_w_down': _jnp.float32}
MOMENT_SCALE = {'norm_mix_g': 9.771910e-02, 'norm_xa_g': 1.232862e-02, 'norm_ffn_g': 1.056304e-01, 'norm_mem_g': 2.657083e-02, 'norm_final_g': 1.605574e+01, 'w_in_ab': 6.788633e-02, 'conv_qkv_a': 6.710178e-02, 'a_log_a': 2.541891e-01, 'dt_bias_a': 2.445813e-01, 'onorm_g_a': 1.789198e-01, 'ssm_lambda_re': 2.443069e-03, 'ssm_lambda_im': 2.891703e-03, 'ssm_b_re': 1.834656e-03, 'ssm_b_im': 1.820809e-03, 'ssm_c_re': 3.588525e-03, 'ssm_c_im': 3.569615e-03, 'ssm_d': 5.489979e-02, 'ssm_log_dt': 1.766796e+00, 'w_glu_b': 1.430322e-02, 'b_glu_b': 2.276434e-02, 'w_out_ab': 6.983240e-02, 'pool_w': 8.010720e-02, 'pool_scale': 3.456847e-01, 'xa_wq': 1.251776e-02, 'xa_wkv': 1.244562e-02, 'xa_wo': 1.252220e-02, 'ffn_w_up': 4.205543e-02, 'ffn_conv': 3.878464e-02, 'ffn_w_down': 6.899978e-02}


def _to_microbatches(a, axis):
    t = _jnp.moveaxis(a, axis, 0)
    t = t.reshape((N_MICROBATCH, t.shape[0] // N_MICROBATCH) + t.shape[1:])
    return _jnp.moveaxis(t, 1, axis + 1)


def setup_inputs(seed: int = 0) -> dict:
    inp = _fwd_setup_inputs(seed)
    key = _jax.random.fold_in(_jax.random.key(seed), 7919)
    shape, _ = _output_shape()
    out = dict(inp)
    out["loss_target"] = _jax.random.normal(_jax.random.fold_in(key, 0), shape, _jnp.float32)
    for i, name in enumerate(TWIN_WEIGHTS):
        w = inp[name].astype(_jnp.float32)
        if MOMENT_SCALE is None:
            s = _jnp.sqrt(_jnp.mean(_jnp.square(w)) + 1e-30)
        else:
            s = MOMENT_SCALE[name]
        km, kv = _jax.random.split(_jax.random.fold_in(key, i + 1))
        out[name] = w
        out["m_" + name] = s * _jax.random.normal(km, w.shape, _jnp.float32)
        out["v_" + name] = (s * s) * _jax.random.uniform(kv, w.shape, _jnp.float32, 0.5, 1.5)
    if N_MICROBATCH > 1:
        for name, axis in PER_EXAMPLE_BATCH_AXIS.items():
            out[name] = _to_microbatches(out[name], axis)
    return {'x': out['x'], 'mem': out['mem'], 'norm_mix_g': out['norm_mix_g'], 'norm_xa_g': out['norm_xa_g'], 'norm_ffn_g': out['norm_ffn_g'], 'norm_mem_g': out['norm_mem_g'], 'norm_final_g': out['norm_final_g'], 'w_in_ab': out['w_in_ab'], 'conv_qkv_a': out['conv_qkv_a'], 'a_log_a': out['a_log_a'], 'dt_bias_a': out['dt_bias_a'], 'onorm_g_a': out['onorm_g_a'], 'ssm_lambda_re': out['ssm_lambda_re'], 'ssm_lambda_im': out['ssm_lambda_im'], 'ssm_b_re': out['ssm_b_re'], 'ssm_b_im': out['ssm_b_im'], 'ssm_c_re': out['ssm_c_re'], 'ssm_c_im': out['ssm_c_im'], 'ssm_d': out['ssm_d'], 'ssm_log_dt': out['ssm_log_dt'], 'w_glu_b': out['w_glu_b'], 'b_glu_b': out['b_glu_b'], 'w_out_ab': out['w_out_ab'], 'pool_w': out['pool_w'], 'pool_scale': out['pool_scale'], 'xa_wq': out['xa_wq'], 'xa_wkv': out['xa_wkv'], 'xa_wo': out['xa_wo'], 'ffn_w_up': out['ffn_w_up'], 'ffn_conv': out['ffn_conv'], 'ffn_w_down': out['ffn_w_down'], 'loss_target': out['loss_target'], 'm_norm_mix_g': out['m_norm_mix_g'], 'm_norm_xa_g': out['m_norm_xa_g'], 'm_norm_ffn_g': out['m_norm_ffn_g'], 'm_norm_mem_g': out['m_norm_mem_g'], 'm_norm_final_g': out['m_norm_final_g'], 'm_w_in_ab': out['m_w_in_ab'], 'm_conv_qkv_a': out['m_conv_qkv_a'], 'm_a_log_a': out['m_a_log_a'], 'm_dt_bias_a': out['m_dt_bias_a'], 'm_onorm_g_a': out['m_onorm_g_a'], 'm_ssm_lambda_re': out['m_ssm_lambda_re'], 'm_ssm_lambda_im': out['m_ssm_lambda_im'], 'm_ssm_b_re': out['m_ssm_b_re'], 'm_ssm_b_im': out['m_ssm_b_im'], 'm_ssm_c_re': out['m_ssm_c_re'], 'm_ssm_c_im': out['m_ssm_c_im'], 'm_ssm_d': out['m_ssm_d'], 'm_ssm_log_dt': out['m_ssm_log_dt'], 'm_w_glu_b': out['m_w_glu_b'], 'm_b_glu_b': out['m_b_glu_b'], 'm_w_out_ab': out['m_w_out_ab'], 'm_pool_w': out['m_pool_w'], 'm_pool_scale': out['m_pool_scale'], 'm_xa_wq': out['m_xa_wq'], 'm_xa_wkv': out['m_xa_wkv'], 'm_xa_wo': out['m_xa_wo'], 'm_ffn_w_up': out['m_ffn_w_up'], 'm_ffn_conv': out['m_ffn_conv'], 'm_ffn_w_down': out['m_ffn_w_down'], 'v_norm_mix_g': out['v_norm_mix_g'], 'v_norm_xa_g': out['v_norm_xa_g'], 'v_norm_ffn_g': out['v_norm_ffn_g'], 'v_norm_mem_g': out['v_norm_mem_g'], 'v_norm_final_g': out['v_norm_final_g'], 'v_w_in_ab': out['v_w_in_ab'], 'v_conv_qkv_a': out['v_conv_qkv_a'], 'v_a_log_a': out['v_a_log_a'], 'v_dt_bias_a': out['v_dt_bias_a'], 'v_onorm_g_a': out['v_onorm_g_a'], 'v_ssm_lambda_re': out['v_ssm_lambda_re'], 'v_ssm_lambda_im': out['v_ssm_lambda_im'], 'v_ssm_b_re': out['v_ssm_b_re'], 'v_ssm_b_im': out['v_ssm_b_im'], 'v_ssm_c_re': out['v_ssm_c_re'], 'v_ssm_c_im': out['v_ssm_c_im'], 'v_ssm_d': out['v_ssm_d'], 'v_ssm_log_dt': out['v_ssm_log_dt'], 'v_w_glu_b': out['v_w_glu_b'], 'v_b_glu_b': out['v_b_glu_b'], 'v_w_out_ab': out['v_w_out_ab'], 'v_pool_w': out['v_pool_w'], 'v_pool_scale': out['v_pool_scale'], 'v_xa_wq': out['v_xa_wq'], 'v_xa_wkv': out['v_xa_wkv'], 'v_xa_wo': out['v_xa_wo'], 'v_ffn_w_up': out['v_ffn_w_up'], 'v_ffn_conv': out['v_ffn_conv'], 'v_ffn_w_down': out['v_ffn_w_down']}


def _loss(weights, diff, rest, loss_target):
    with _jax.named_scope("forward"):
        args = {**rest, TWIN_DIFF_INPUT: diff, **{k: w.astype(_WEIGHT_DTYPES[k]) for k, w in weights.items()}}
        y = _forward(args)
    with _jax.named_scope("loss_head"):
        err = _jnp.square(y.astype(_jnp.float32) - loss_target)
        return 0.5 * _jnp.sum(_jnp.mean(err, axis=-1)) if err.ndim else 0.5 * err


def _adamw(w, g, m, v):
    m = ADAM_B1 * m + (1.0 - ADAM_B1) * g
    v = ADAM_B2 * v + (1.0 - ADAM_B2) * _jnp.square(g)
    m_hat = m / (1.0 - ADAM_B1 ** ADAM_STEP)
    v_hat = v / (1.0 - ADAM_B2 ** ADAM_STEP)
    delta = -ADAM_LR * (m_hat / (_jnp.sqrt(v_hat) + ADAM_EPS) + ADAM_WD * w)
    return delta, m, v


def reference(x, mem, norm_mix_g, norm_xa_g, norm_ffn_g, norm_mem_g, norm_final_g, w_in_ab, conv_qkv_a, a_log_a, dt_bias_a, onorm_g_a, ssm_lambda_re, ssm_lambda_im, ssm_b_re, ssm_b_im, ssm_c_re, ssm_c_im, ssm_d, ssm_log_dt, w_glu_b, b_glu_b, w_out_ab, pool_w, pool_scale, xa_wq, xa_wkv, xa_wo, ffn_w_up, ffn_conv, ffn_w_down, loss_target, m_norm_mix_g, m_norm_xa_g, m_norm_ffn_g, m_norm_mem_g, m_norm_final_g, m_w_in_ab, m_conv_qkv_a, m_a_log_a, m_dt_bias_a, m_onorm_g_a, m_ssm_lambda_re, m_ssm_lambda_im, m_ssm_b_re, m_ssm_b_im, m_ssm_c_re, m_ssm_c_im, m_ssm_d, m_ssm_log_dt, m_w_glu_b, m_b_glu_b, m_w_out_ab, m_pool_w, m_pool_scale, m_xa_wq, m_xa_wkv, m_xa_wo, m_ffn_w_up, m_ffn_conv, m_ffn_w_down, v_norm_mix_g, v_norm_xa_g, v_norm_ffn_g, v_norm_mem_g, v_norm_final_g, v_w_in_ab, v_conv_qkv_a, v_a_log_a, v_dt_bias_a, v_onorm_g_a, v_ssm_lambda_re, v_ssm_lambda_im, v_ssm_b_re, v_ssm_b_im, v_ssm_c_re, v_ssm_c_im, v_ssm_d, v_ssm_log_dt, v_w_glu_b, v_b_glu_b, v_w_out_ab, v_pool_w, v_pool_scale, v_xa_wq, v_xa_wkv, v_xa_wo, v_ffn_w_up, v_ffn_conv, v_ffn_w_down):
    given = dict(x=x, mem=mem, norm_mix_g=norm_mix_g, norm_xa_g=norm_xa_g, norm_ffn_g=norm_ffn_g, norm_mem_g=norm_mem_g, norm_final_g=norm_final_g, w_in_ab=w_in_ab, conv_qkv_a=conv_qkv_a, a_log_a=a_log_a, dt_bias_a=dt_bias_a, onorm_g_a=onorm_g_a, ssm_lambda_re=ssm_lambda_re, ssm_lambda_im=ssm_lambda_im, ssm_b_re=ssm_b_re, ssm_b_im=ssm_b_im, ssm_c_re=ssm_c_re, ssm_c_im=ssm_c_im, ssm_d=ssm_d, ssm_log_dt=ssm_log_dt, w_glu_b=w_glu_b, b_glu_b=b_glu_b, w_out_ab=w_out_ab, pool_w=pool_w, pool_scale=pool_scale, xa_wq=xa_wq, xa_wkv=xa_wkv, xa_wo=xa_wo, ffn_w_up=ffn_w_up, ffn_conv=ffn_conv, ffn_w_down=ffn_w_down, loss_target=loss_target, m_norm_mix_g=m_norm_mix_g, m_norm_xa_g=m_norm_xa_g, m_norm_ffn_g=m_norm_ffn_g, m_norm_mem_g=m_norm_mem_g, m_norm_final_g=m_norm_final_g, m_w_in_ab=m_w_in_ab, m_conv_qkv_a=m_conv_qkv_a, m_a_log_a=m_a_log_a, m_dt_bias_a=m_dt_bias_a, m_onorm_g_a=m_onorm_g_a, m_ssm_lambda_re=m_ssm_lambda_re, m_ssm_lambda_im=m_ssm_lambda_im, m_ssm_b_re=m_ssm_b_re, m_ssm_b_im=m_ssm_b_im, m_ssm_c_re=m_ssm_c_re, m_ssm_c_im=m_ssm_c_im, m_ssm_d=m_ssm_d, m_ssm_log_dt=m_ssm_log_dt, m_w_glu_b=m_w_glu_b, m_b_glu_b=m_b_glu_b, m_w_out_ab=m_w_out_ab, m_pool_w=m_pool_w, m_pool_scale=m_pool_scale, m_xa_wq=m_xa_wq, m_xa_wkv=m_xa_wkv, m_xa_wo=m_xa_wo, m_ffn_w_up=m_ffn_w_up, m_ffn_conv=m_ffn_conv, m_ffn_w_down=m_ffn_w_down, v_norm_mix_g=v_norm_mix_g, v_norm_xa_g=v_norm_xa_g, v_norm_ffn_g=v_norm_ffn_g, v_norm_mem_g=v_norm_mem_g, v_norm_final_g=v_norm_final_g, v_w_in_ab=v_w_in_ab, v_conv_qkv_a=v_conv_qkv_a, v_a_log_a=v_a_log_a, v_dt_bias_a=v_dt_bias_a, v_onorm_g_a=v_onorm_g_a, v_ssm_lambda_re=v_ssm_lambda_re, v_ssm_lambda_im=v_ssm_lambda_im, v_ssm_b_re=v_ssm_b_re, v_ssm_b_im=v_ssm_b_im, v_ssm_c_re=v_ssm_c_re, v_ssm_c_im=v_ssm_c_im, v_ssm_d=v_ssm_d, v_ssm_log_dt=v_ssm_log_dt, v_w_glu_b=v_w_glu_b, v_b_glu_b=v_b_glu_b, v_w_out_ab=v_w_out_ab, v_pool_w=v_pool_w, v_pool_scale=v_pool_scale, v_xa_wq=v_xa_wq, v_xa_wkv=v_xa_wkv, v_xa_wo=v_xa_wo, v_ffn_w_up=v_ffn_w_up, v_ffn_conv=v_ffn_conv, v_ffn_w_down=v_ffn_w_down)
    weights = {n: given[n] for n in TWIN_WEIGHTS}
    shared = {n: given[n] for n in SHARED_INPUTS}
    per_example = {n: given[n] for n in ['x', 'mem']}
    grad_fn = _jax.value_and_grad(_loss, argnums=(0, 1))

    def one_microbatch(ex, loss_target):
        ex = dict(ex)
        diff = ex.pop(TWIN_DIFF_INPUT)
        return grad_fn(weights, diff, {**shared, **ex}, loss_target)

    if N_MICROBATCH == 1:
        loss, (grad_w, grad_x) = one_microbatch(per_example, given["loss_target"])
    else:
        def body(carry, xs):
            loss_sum, grad_sum = carry
            l_k, (gw_k, gx_k) = one_microbatch(xs[0], xs[1])
            with _jax.named_scope("update"):
                return (loss_sum + l_k, _jax.tree.map(_jnp.add, grad_sum, gw_k)), gx_k

        init = (_jnp.zeros((), _jnp.float32), _jax.tree.map(_jnp.zeros_like, weights))
        (loss, grad_w), grad_x = _jax.lax.scan(body, init, (per_example, given["loss_target"]))
    with _jax.named_scope("update"):
        delta_w, new_m, new_v = {}, {}, {}
        for n in TWIN_WEIGHTS:
            delta_w[n], new_m[n], new_v[n] = _adamw(weights[n], grad_w[n], given["m_" + n], given["v_" + n])
    return (loss, grad_x, *[grad_w[n] for n in TWIN_WEIGHTS], *[delta_w[n] for n in TWIN_WEIGHTS],
            *[new_m[n] for n in TWIN_WEIGHTS], *[new_v[n] for n in TWIN_WEIGHTS])
```

```python
import functools
import math

import numpy as np
import jax
import jax.numpy as jnp
from jax import lax
from jax.experimental import pallas as pl
from jax.experimental.pallas import tpu as pltpu

F32, BF16 = jnp.float32, jnp.bfloat16
HIGH, HIGHEST = lax.Precision.HIGH, lax.Precision.HIGHEST
MESH = pl.DeviceIdType.MESH

N_DEV = 8
SEQ, D_MODEL, MEM_LEN = 2048, 1024, 256
WIDTH_A, N_HEADS_A, HEAD_A, CONV_A = 512, 4, 128, 4
GDR_CHUNK = 128
SSM_WIDTH, SSM_GROUP, N_GROUPS, SSM_STATE = 512, 16, 32, 64
SSM_CH = N_GROUPS * SSM_STATE
SCAN_CB = 512
POOL_WINDOWS = (2, 4, 8, 16)
POOL_GROUP = 256
N_HEADS_X, HEAD_X = 4, 256
D_FF, CONV_FFN = 2816, 3
RMS_EPS = 1e-6
ADAM_LR, ADAM_B1, ADAM_B2, ADAM_EPS, ADAM_WD, ADAM_STEP = 0.001, 0.9, 0.999, 1e-08, 0.01, 10
LANE = 128
PACK_COLS = 1024
VMEM_LIMIT_BYTES = 56 * 1024 * 1024


def _params(sem=None):
    return pltpu.CompilerParams(dimension_semantics=sem, vmem_limit_bytes=VMEM_LIMIT_BYTES)


def _tile(dim, pref):
    best = None
    for t in range(LANE, min(dim, pref) + 1, LANE):
        if dim % t == 0:
            best = t
    return best if best is not None else dim


def _mm(a, b, mode, name, out_dtype=F32, res=None, tm=512, tn=512, tk=1024):
    if mode == "nn":
        (m, k), n = a.shape, b.shape[1]
    elif mode == "nt":
        (m, k), n = a.shape, b.shape[0]
    else:
        (k, m), n = a.shape, b.shape[1]
    tm, tn, tk = _tile(m, tm), _tile(n, tn), _tile(k, tk)
    nk = k // tk
    dims = {"nn": ((1,), (0,)), "nt": ((1,), (1,)), "tn": ((0,), (0,))}[mode]

    def body(*refs):
        if res is None:
            a_ref, b_ref, o_ref, acc = refs
            r_ref = None
        else:
            a_ref, b_ref, r_ref, o_ref, acc = refs
        kk = pl.program_id(2)

        @pl.when(kk == 0)
        def _():
            acc[...] = jnp.zeros_like(acc)

        acc[...] += lax.dot_general(a_ref[...].astype(BF16), b_ref[...].astype(BF16), (dims, ((), ())),
                                    preferred_element_type=F32)

        @pl.when(kk == nk - 1)
        def _():
            out = acc[...]
            if r_ref is not None:
                out = out + r_ref[...].astype(F32)
            o_ref[...] = out.astype(out_dtype)

    a_spec = (pl.BlockSpec((tk, tm), lambda i, j, q: (q, i)) if mode == "tn"
              else pl.BlockSpec((tm, tk), lambda i, j, q: (i, q)))
    b_spec = (pl.BlockSpec((tn, tk), lambda i, j, q: (j, q)) if mode == "nt"
              else pl.BlockSpec((tk, tn), lambda i, j, q: (q, j)))
    o_spec = pl.BlockSpec((tm, tn), lambda i, j, q: (i, j))
    in_specs, args = [a_spec, b_spec], [a, b]
    if res is not None:
        in_specs.append(o_spec)
        args.append(res)
    return pl.pallas_call(
        body, name=name, grid=(m // tm, n // tn, nk), in_specs=in_specs, out_specs=o_spec,
        out_shape=jax.ShapeDtypeStruct((m, n), out_dtype), scratch_shapes=[pltpu.VMEM((tm, tn), F32)],
        compiler_params=_params(("parallel", "parallel", "arbitrary")))(*args)


def _rms_fwd(x, g, out_dtype, name, tr=256):
    rows, d = x.shape

    def body(x_ref, g_ref, o_ref):
        xv = x_ref[...]
        r = lax.rsqrt(jnp.mean(xv * xv, axis=-1, keepdims=True) + RMS_EPS)
        o_ref[...] = (xv * r * g_ref[...]).astype(out_dtype)

    return pl.pallas_call(
        body, name=name, grid=(rows // tr,),
        in_specs=[pl.BlockSpec((tr, d), lambda i: (i, 0)), pl.BlockSpec((1, d), lambda i: (0, 0))],
        out_specs=pl.BlockSpec((tr, d), lambda i: (i, 0)), out_shape=jax.ShapeDtypeStruct((rows, d), out_dtype),
        compiler_params=_params(("parallel",)))(x, g.reshape(1, d))


def _rms_bwd(x, g, dy, dres, name, tr=256):
    rows, d = x.shape

    def body(*refs):
        if dres is None:
            x_ref, g_ref, dy_ref, dx_ref, dg_ref = refs
            r_ref = None
        else:
            x_ref, g_ref, dy_ref, r_ref, dx_ref, dg_ref = refs

        @pl.when(pl.program_id(0) == 0)
        def _():
            dg_ref[...] = jnp.zeros_like(dg_ref)

        xv, dyv = x_ref[...], dy_ref[...].astype(F32)
        r = lax.rsqrt(jnp.mean(xv * xv, axis=-1, keepdims=True) + RMS_EPS)
        xh = xv * r
        dyg = dyv * g_ref[...]
        dx = r * (dyg - xh * jnp.mean(dyg * xh, axis=-1, keepdims=True))
        if r_ref is not None:
            dx = dx + r_ref[...]
        dx_ref[...] = dx
        dg_ref[...] += jnp.sum(dyv * xh, axis=0, keepdims=True)

    blk = pl.BlockSpec((tr, d), lambda i: (i, 0))
    vec = pl.BlockSpec((1, d), lambda i: (0, 0))
    in_specs, args = [blk, vec, blk], [x, g.reshape(1, d), dy]
    if dres is not None:
        in_specs.append(blk)
        args.append(dres)
    return pl.pallas_call(
        body, name=name, grid=(rows // tr,), in_specs=in_specs, out_specs=[blk, vec],
        out_shape=[jax.ShapeDtypeStruct((rows, d), F32), jax.ShapeDtypeStruct((1, d), F32)],
        compiler_params=_params(("arbitrary",)))(*args)


def _loss_head(x, g, target, name, tr=256):
    rows, d = x.shape

    def body(x_ref, g_ref, t_ref, loss_ref, dx_ref, dg_ref):
        @pl.when(pl.program_id(0) == 0)
        def _():
            dg_ref[...] = jnp.zeros_like(dg_ref)
            loss_ref[...] = jnp.zeros_like(loss_ref)

        xv = x_ref[...]
        r = lax.rsqrt(jnp.mean(xv * xv, axis=-1, keepdims=True) + RMS_EPS)
        xh = xv * r
        err = xh * g_ref[...] - t_ref[...]
        loss_ref[...] += 0.5 * jnp.sum(jnp.mean(err * err, axis=-1, keepdims=True), keepdims=True)
        dyv = err * (1.0 / d)
        dyg = dyv * g_ref[...]
        dx_ref[...] = r * (dyg - xh * jnp.mean(dyg * xh, axis=-1, keepdims=True))
        dg_ref[...] += jnp.sum(dyv * xh, axis=0, keepdims=True)

    blk = pl.BlockSpec((tr, d), lambda i: (i, 0))
    vec = pl.BlockSpec((1, d), lambda i: (0, 0))
    return pl.pallas_call(
        body, name=name, grid=(rows // tr,), in_specs=[blk, vec, blk],
        out_specs=[pl.BlockSpec((1, 1), lambda i: (0, 0)), blk, vec],
        out_shape=[jax.ShapeDtypeStruct((1, 1), F32), jax.ShapeDtypeStruct((rows, d), F32),
                   jax.ShapeDtypeStruct((1, d), F32)],
        compiler_params=_params(("arbitrary",)))(x, g.reshape(1, d), target)


def _shift_down(x, s):
    rows = lax.broadcasted_iota(jnp.int32, x.shape, 0)
    return jnp.where(rows >= s, pltpu.roll(x, s, 0), 0.0)


def _shift_up(x, s):
    n = x.shape[0]
    rows = lax.broadcasted_iota(jnp.int32, x.shape, 0)
    return jnp.where(rows < n - s, pltpu.roll(x, n - s, 0), 0.0)


def _sigmoid(x):
    return 1.0 / (1.0 + jnp.exp(-x))


def _silu_and_grad(x):
    s = _sigmoid(x)
    return x * s, s * (1.0 + x * (1.0 - s))


_GELU_C0, _GELU_C1 = math.sqrt(2.0 / math.pi), 0.044715


def _gelu_and_grad(x):
    th = jnp.tanh(_GELU_C0 * (x + _GELU_C1 * x * x * x))
    y = 0.5 * x * (1.0 + th)
    dy = 0.5 * (1.0 + th) + 0.5 * x * (1.0 - th * th) * _GELU_C0 * (1.0 + 3.0 * _GELU_C1 * x * x)
    return y, dy


def _ffn_act_fwd(h, w, name, tc=256):
    t = h.shape[0]
    nb = D_FF // tc

    def body(hg_ref, hv_ref, wg_ref, wv_ref, a_ref):
        def conv(x, wr):
            return wr[2:3, :] * x + wr[1:2, :] * _shift_down(x, 1) + wr[0:1, :] * _shift_down(x, 2)

        cg = conv(hg_ref[...], wg_ref[...])
        cv = conv(hv_ref[...], wv_ref[...])
        a_ref[...] = (cg * _sigmoid(cg) * cv).astype(BF16)

    return pl.pallas_call(
        body, name=name, grid=(nb,),
        in_specs=[pl.BlockSpec((t, tc), lambda j: (0, j)), pl.BlockSpec((t, tc), lambda j: (0, j + nb)),
                  pl.BlockSpec((CONV_FFN, tc), lambda j: (0, j)), pl.BlockSpec((CONV_FFN, tc), lambda j: (0, j + nb))],
        out_specs=pl.BlockSpec((t, tc), lambda j: (0, j)), out_shape=jax.ShapeDtypeStruct((t, D_FF), BF16),
        compiler_params=_params(("parallel",)))(h, h, w, w)


def _ffn_act_bwd(h, w, da, name, tc=256):
    t = h.shape[0]
    nb = D_FF // tc

    def body(hg_ref, hv_ref, wg_ref, wv_ref, da_ref, dhg_ref, dhv_ref, dwg_ref, dwv_ref):
        hg, hv, wg, wv = hg_ref[...], hv_ref[...], wg_ref[...], wv_ref[...]
        hg1, hg2, hv1, hv2 = _shift_down(hg, 1), _shift_down(hg, 2), _shift_down(hv, 1), _shift_down(hv, 2)
        cg = wg[2:3, :] * hg + wg[1:2, :] * hg1 + wg[0:1, :] * hg2
        cv = wv[2:3, :] * hv + wv[1:2, :] * hv1 + wv[0:1, :] * hv2
        sg, dsg = _silu_and_grad(cg)
        dav = da_ref[...].astype(F32)
        dcv = dav * sg
        dcg = dav * cv * dsg

        def conv_t(dc, wr):
            return wr[2:3, :] * dc + wr[1:2, :] * _shift_up(dc, 1) + wr[0:1, :] * _shift_up(dc, 2)

        dhg_ref[...] = conv_t(dcg, wg).astype(BF16)
        dhv_ref[...] = conv_t(dcv, wv).astype(BF16)
        dwg_ref[0:1, :] = jnp.sum(dcg * hg2, axis=0, keepdims=True)
        dwg_ref[1:2, :] = jnp.sum(dcg * hg1, axis=0, keepdims=True)
        dwg_ref[2:3, :] = jnp.sum(dcg * hg, axis=0, keepdims=True)
        dwv_ref[0:1, :] = jnp.sum(dcv * hv2, axis=0, keepdims=True)
        dwv_ref[1:2, :] = jnp.sum(dcv * hv1, axis=0, keepdims=True)
        dwv_ref[2:3, :] = jnp.sum(dcv * hv, axis=0, keepdims=True)

    big = lambda off: pl.BlockSpec((t, tc), lambda j: (0, j + off))
    small = lambda off: pl.BlockSpec((CONV_FFN, tc), lambda j: (0, j + off))
    dhg, dhv, dwg, dwv = pl.pallas_call(
        body, name=name, grid=(nb,),
        in_specs=[big(0), big(nb), small(0), small(nb), big(0)],
        out_specs=[big(0), big(0), small(0), small(0)],
        out_shape=[jax.ShapeDtypeStruct((t, D_FF), BF16), jax.ShapeDtypeStruct((t, D_FF), BF16),
                   jax.ShapeDtypeStruct((CONV_FFN, D_FF), F32), jax.ShapeDtypeStruct((CONV_FFN, D_FF), F32)],
        compiler_params=_params(("parallel",)))(h, h, w, w, da)
    return jnp.concatenate([dhg, dhv], axis=1), jnp.concatenate([dwg, dwv], axis=1)


def _attn_probs(q, k):
    s = lax.dot_general(q.astype(BF16), k.astype(BF16), (((1,), (1,)), ((), ())),
                        preferred_element_type=F32) * (HEAD_X ** -0.5)
    s = s - jnp.max(s, axis=-1, keepdims=True)
    p = jnp.exp(s)
    return p / jnp.sum(p, axis=-1, keepdims=True)


def _attn_fwd(q, kv, name, tq=512):
    t = q.shape[0]

    def body(q_ref, k_ref, v_ref, o_ref):
        p = _attn_probs(q_ref[...], k_ref[...])
        o_ref[...] = jnp.dot(p.astype(BF16), v_ref[...].astype(BF16), preferred_element_type=F32).astype(BF16)

    return pl.pallas_call(
        body, name=name, grid=(N_HEADS_X, t // tq),
        in_specs=[pl.BlockSpec((tq, HEAD_X), lambda h, i: (i, h)),
                  pl.BlockSpec((MEM_LEN, HEAD_X), lambda h, i: (0, h)),
                  pl.BlockSpec((MEM_LEN, HEAD_X), lambda h, i: (0, h + N_HEADS_X))],
        out_specs=pl.BlockSpec((tq, HEAD_X), lambda h, i: (i, h)),
        out_shape=jax.ShapeDtypeStruct((t, N_HEADS_X * HEAD_X), BF16),
        compiler_params=_params(("parallel", "parallel")))(q, kv, kv)


def _attn_bwd(q, kv, do, name, tq=512):
    t = q.shape[0]

    def body(q_ref, k_ref, v_ref, do_ref, dq_ref, dk_ref, dv_ref):
        @pl.when(pl.program_id(1) == 0)
        def _():
            dk_ref[...] = jnp.zeros_like(dk_ref)
            dv_ref[...] = jnp.zeros_like(dv_ref)

        qb, kb, vb, dob = (r[...].astype(BF16) for r in (q_ref, k_ref, v_ref, do_ref))
        p = _attn_probs(qb, kb)
        dp = lax.dot_general(dob, vb, (((1,), (1,)), ((), ())), preferred_element_type=F32)
        ds = p * (dp - jnp.sum(dp * p, axis=-1, keepdims=True)) * (HEAD_X ** -0.5)
        dsb = ds.astype(BF16)
        dq_ref[...] = jnp.dot(dsb, kb, preferred_element_type=F32).astype(BF16)
        dk_ref[...] += lax.dot_general(dsb, qb, (((0,), (0,)), ((), ())), preferred_element_type=F32)
        dv_ref[...] += lax.dot_general(p.astype(BF16), dob, (((0,), (0,)), ((), ())), preferred_element_type=F32)

    qs = pl.BlockSpec((tq, HEAD_X), lambda h, i: (i, h))
    ms = pl.BlockSpec((MEM_LEN, HEAD_X), lambda h, i: (0, h))
    return pl.pallas_call(
        body, name=name, grid=(N_HEADS_X, t // tq),
        in_specs=[qs, ms, pl.BlockSpec((MEM_LEN, HEAD_X), lambda h, i: (0, h + N_HEADS_X)), qs],
        out_specs=[qs, ms, ms],
        out_shape=[jax.ShapeDtypeStruct((t, D_MODEL), BF16), jax.ShapeDtypeStruct((MEM_LEN, D_MODEL), F32),
                   jax.ShapeDtypeStruct((MEM_LEN, D_MODEL), F32)],
        compiler_params=_params(("parallel", "arbitrary")))(q, kv, kv, do)


def _pool_counts(t, win):
    pos = lax.broadcasted_iota(jnp.int32, (t, 1), 0).astype(F32) + 1.0
    return 1.0 / jnp.minimum(pos, float(win))


def _pool_delta(xv, win):
    s, step = xv, 1
    while step < win:
        s = s + _shift_down(s, step)
        step *= 2
    return s * _pool_counts(xv.shape[0], win) - xv


def _pool_delta_t(dv, win):
    s, step = dv * _pool_counts(dv.shape[0], win), 1
    while step < win:
        s = s + _shift_up(s, step)
        step *= 2
    return s - dv


def _pool_fwd(xn, w, scale, res, name):
    t = xn.shape[0]

    def make_branch(win, xn_ref, w_ref, s_ref, r_ref, o_ref):
        def branch():
            dl = _pool_delta(xn_ref[...], win)
            y = jnp.dot(dl.astype(BF16), w_ref[0], preferred_element_type=F32)
            o_ref[...] = r_ref[...] + y * s_ref[...]
        return branch

    def body(xn_ref, w_ref, s_ref, r_ref, o_ref):
        for gi, win in enumerate(POOL_WINDOWS):
            pl.when(pl.program_id(0) == gi)(make_branch(win, xn_ref, w_ref, s_ref, r_ref, o_ref))

    blk = pl.BlockSpec((t, POOL_GROUP), lambda g: (0, g))
    return pl.pallas_call(
        body, name=name, grid=(len(POOL_WINDOWS),),
        in_specs=[blk, pl.BlockSpec((1, POOL_GROUP, POOL_GROUP), lambda g: (g, 0, 0)),
                  pl.BlockSpec((1, POOL_GROUP), lambda g: (0, g)), blk],
        out_specs=blk, out_shape=jax.ShapeDtypeStruct((t, D_MODEL), F32),
        compiler_params=_params(("parallel",)))(xn, w, scale, res)


def _pool_bwd(xn, w, scale, dmix, name):
    t = xn.shape[0]

    def make_branch(win, xn_ref, w_ref, s_ref, d_ref, dxn_ref, dw_ref, ds_ref):
        def branch():
            dl = _pool_delta(xn_ref[...], win).astype(BF16)
            wv = w_ref[0]
            dm = d_ref[...]
            y = jnp.dot(dl, wv, preferred_element_type=F32)
            ds_ref[...] = jnp.sum(dm * y, axis=0, keepdims=True)
            dy = (dm * s_ref[...]).astype(BF16)
            dw_ref[0] = lax.dot_general(dl, dy, (((0,), (0,)), ((), ())), preferred_element_type=F32)
            ddl = lax.dot_general(dy, wv, (((1,), (1,)), ((), ())), preferred_element_type=F32)
            dxn_ref[...] = _pool_delta_t(ddl, win)
        return branch

    def body(*refs):
        for gi, win in enumerate(POOL_WINDOWS):
            pl.when(pl.program_id(0) == gi)(make_branch(win, *refs))

    blk = pl.BlockSpec((t, POOL_GROUP), lambda g: (0, g))
    wspec = pl.BlockSpec((1, POOL_GROUP, POOL_GROUP), lambda g: (g, 0, 0))
    vec = pl.BlockSpec((1, POOL_GROUP), lambda g: (0, g))
    return pl.pallas_call(
        body, name=name, grid=(len(POOL_WINDOWS),), in_specs=[blk, wspec, vec, blk], out_specs=[blk, wspec, vec],
        out_shape=[jax.ShapeDtypeStruct((t, D_MODEL), F32),
                   jax.ShapeDtypeStruct((len(POOL_WINDOWS), POOL_GROUP, POOL_GROUP), F32),
                   jax.ShapeDtypeStruct((1, D_MODEL), F32)],
        compiler_params=_params(("parallel",)))(xn, w, scale, dmix)


def _qkv_conv(h, wr):
    return (wr[3:4, :] * h + wr[2:3, :] * _shift_down(h, 1) + wr[1:2, :] * _shift_down(h, 2)
            + wr[0:1, :] * _shift_down(h, 3))


def _qkv_pre_fwd(h, w, col0, ncols, normalize, scale, name):
    t = h.shape[0]

    def body(h_ref, w_ref, o_ref):
        c = _qkv_conv(h_ref[...], w_ref[...])
        s = c * _sigmoid(c)
        if normalize:
            s = s * lax.rsqrt(jnp.sum(s * s, axis=-1, keepdims=True) + 1e-6) * scale
        o_ref[...] = s

    return pl.pallas_call(
        body, name=name, grid=(ncols,),
        in_specs=[pl.BlockSpec((t, HEAD_A), lambda j: (0, j + col0)), pl.BlockSpec((CONV_A, HEAD_A), lambda j: (0, j + col0))],
        out_specs=pl.BlockSpec((t, HEAD_A), lambda j: (0, j)), out_shape=jax.ShapeDtypeStruct((t, ncols * HEAD_A), F32),
        compiler_params=_params(("parallel",)))(h, w)


def _qkv_pre_bwd(h, w, dy, col0, ncols, normalize, scale, name):
    t = h.shape[0]

    def body(h_ref, w_ref, dy_ref, dh_ref, dw_ref):
        hv, wr, dyv = h_ref[...], w_ref[...], dy_ref[...]
        h1, h2, h3 = _shift_down(hv, 1), _shift_down(hv, 2), _shift_down(hv, 3)
        c = wr[3:4, :] * hv + wr[2:3, :] * h1 + wr[1:2, :] * h2 + wr[0:1, :] * h3
        s, dsilu = _silu_and_grad(c)
        if normalize:
            r = lax.rsqrt(jnp.sum(s * s, axis=-1, keepdims=True) + 1e-6)
            y = s * r
            dyv = dyv * scale
            ds = r * (dyv - y * jnp.sum(dyv * y, axis=-1, keepdims=True))
        else:
            ds = dyv
        dc = ds * dsilu
        dh = (wr[3:4, :] * dc + wr[2:3, :] * _shift_up(dc, 1) + wr[1:2, :] * _shift_up(dc, 2)
              + wr[0:1, :] * _shift_up(dc, 3))
        dh_ref[...] = dh.astype(BF16)
        dw_ref[0:1, :] = jnp.sum(dc * h3, axis=0, keepdims=True)
        dw_ref[1:2, :] = jnp.sum(dc * h2, axis=0, keepdims=True)
        dw_ref[2:3, :] = jnp.sum(dc * h1, axis=0, keepdims=True)
        dw_ref[3:4, :] = jnp.sum(dc * hv, axis=0, keepdims=True)

    return pl.pallas_call(
        body, name=name, grid=(ncols,),
        in_specs=[pl.BlockSpec((t, HEAD_A), lambda j: (0, j + col0)), pl.BlockSpec((CONV_A, HEAD_A), lambda j: (0, j + col0)),
                  pl.BlockSpec((t, HEAD_A), lambda j: (0, j))],
        out_specs=[pl.BlockSpec((t, HEAD_A), lambda j: (0, j)), pl.BlockSpec((CONV_A, HEAD_A), lambda j: (0, j))],
        out_shape=[jax.ShapeDtypeStruct((t, ncols * HEAD_A), BF16), jax.ShapeDtypeStruct((CONV_A, ncols * HEAD_A), F32)],
        compiler_params=_params(("parallel",)))(h, w, dy)


def _softplus(x):
    return jnp.maximum(x, 0.0) + jnp.log1p(jnp.exp(-jnp.abs(x)))


def _gates_fwd(ba, arow, brow, name):
    t = ba.shape[0]

    def body(x_ref, a_ref, b_ref, o_ref):
        xv = x_ref[...]
        lane = lax.broadcasted_iota(jnp.int32, xv.shape, 1)
        beta = _sigmoid(xv)
        g = -jnp.exp(a_ref[...]) * _softplus(xv + b_ref[...])
        o_ref[...] = jnp.where(lane < N_HEADS_A, beta, jnp.where(lane < 2 * N_HEADS_A, g, 0.0))

    return pl.pallas_call(body, name=name, out_shape=jax.ShapeDtypeStruct((t, LANE), F32),
                          compiler_params=_params())(ba, arow, brow)


def _gates_bwd(ba, arow, brow, dgb, name):
    t = ba.shape[0]

    def body(x_ref, a_ref, b_ref, d_ref, dx_ref, da_ref, db_ref):
        xv, dv = x_ref[...], d_ref[...]
        lane = lax.broadcasted_iota(jnp.int32, xv.shape, 1)
        beta = _sigmoid(xv)
        ea = jnp.exp(a_ref[...])
        z = xv + b_ref[...]
        dgv = jnp.where((lane >= N_HEADS_A) & (lane < 2 * N_HEADS_A), dv, 0.0) * (-ea)
        dz = dgv * _sigmoid(z)
        dx = jnp.where(lane < N_HEADS_A, dv * beta * (1.0 - beta), dz)
        dx_ref[...] = dx.astype(BF16)
        db_ref[...] = jnp.sum(dz, axis=0, keepdims=True)
        da_ref[...] = jnp.sum(dgv * _softplus(z), axis=0, keepdims=True)

    return pl.pallas_call(
        body, name=name,
        out_shape=[jax.ShapeDtypeStruct((t, LANE), BF16), jax.ShapeDtypeStruct((1, LANE), F32),
                   jax.ShapeDtypeStruct((1, LANE), F32)],
        compiler_params=_params())(ba, arow, brow, dgb)


def _dot(a, b, prec=None):
    if prec is None:
        return jnp.dot(a.astype(BF16), b.astype(BF16), preferred_element_type=F32)
    return jnp.dot(a, b, precision=prec, preferred_element_type=F32)


def _dot_nt(a, b, prec=None):
    if prec is None:
        a, b = a.astype(BF16), b.astype(BF16)
    return lax.dot_general(a, b, (((1,), (1,)), ((), ())), precision=prec, preferred_element_type=F32)


def _dot_tn(a, b, prec=None):
    if prec is None:
        a, b = a.astype(BF16), b.astype(BF16)
    return lax.dot_general(a, b, (((0,), (0,)), ((), ())), precision=prec, preferred_element_type=F32)


def _gdr_chunk_terms(k, beta, g):
    c = GDR_CHUNK
    row = lax.broadcasted_iota(jnp.int32, (c, c), 0)
    col = lax.broadcasted_iota(jnp.int32, (c, c), 1)
    causal, strict = row >= col, row > col
    gcum = _dot(causal.astype(F32), jnp.broadcast_to(g, (c, c)), HIGHEST)
    diff = gcum - gcum.T
    decay = jnp.where(causal, jnp.exp(jnp.where(causal, diff, 0.0)), 0.0)
    kb = k * beta
    kk = _dot_nt(kb, k)
    return row, col, causal, strict, gcum, decay, kb, kk


def _unit_lower_inverse(a):
    c = a.shape[0]
    eye = (lax.broadcasted_iota(jnp.int32, (c, c), 0) == lax.broadcasted_iota(jnp.int32, (c, c), 1)).astype(F32)
    p = -a
    inv = eye + p
    step = 1
    while 2 * step < c:
        p = _dot(p, p, HIGH)
        inv = inv + _dot(inv, p, HIGH)
        step *= 2
    return inv


def _gdr_fwd(q, k, v, gb, name):
    t = q.shape[0]
    c = GDR_CHUNK
    n = t // c

    def body(q_ref, k_ref, v_ref, gb_ref, o_ref, tm_ref, s_ref, state):
        @pl.when(pl.program_id(1) == 0)
        def _():
            state[...] = jnp.zeros_like(state)

        qv, kv, vv = q_ref[...], k_ref[...], v_ref[...]
        beta, g = gb_ref[0, :, 0:1], gb_ref[0, :, 1:2]
        row, col, causal, strict, gcum, decay, kb, kk = _gdr_chunk_terms(kv, beta, g)
        tm = _unit_lower_inverse(jnp.where(strict, kk * decay, 0.0))
        e = jnp.exp(gcum)
        u = _dot(tm, vv * beta, HIGH)
        w = _dot(tm, kb * e, HIGH)
        p = jnp.where(causal, _dot_nt(qv, kv) * decay, 0.0)
        s = state[...]
        s_ref[0, 0] = s
        tm_ref[0, 0] = tm
        vn = u - _dot(w, s)
        o_ref[...] = _dot(qv * e, s) + _dot(p, vn)
        glast = gcum[c - 1:c, :]
        state[...] = s * jnp.exp(glast) + _dot_tn(kv * jnp.exp(glast - gcum), vn)

    blk = pl.BlockSpec((c, HEAD_A), lambda h, i: (i, h))
    mat = pl.BlockSpec((1, 1, c, c), lambda h, i: (h, i, 0, 0))
    return pl.pallas_call(
        body, name=name, grid=(N_HEADS_A, n),
        in_specs=[blk, blk, blk, pl.BlockSpec((1, c, 2), lambda h, i: (h, i, 0))],
        out_specs=[blk, mat, mat],
        out_shape=[jax.ShapeDtypeStruct((t, WIDTH_A), F32), jax.ShapeDtypeStruct((N_HEADS_A, n, c, c), F32),
                   jax.ShapeDtypeStruct((N_HEADS_A, n, HEAD_A, HEAD_A), F32)],
        scratch_shapes=[pltpu.VMEM((HEAD_A, HEAD_A), F32)],
        compiler_params=_params(("parallel", "arbitrary")))(q, k, v, gb)


def _gdr_bwd(q, k, v, gb, tm_all, s_all, do, name):
    t = q.shape[0]
    c = GDR_CHUNK
    n = t // c

    def body(q_ref, k_ref, v_ref, gb_ref, tm_ref, s_ref, do_ref, dq_ref, dk_ref, dv_ref, dgb_ref, dstate):
        @pl.when(pl.program_id(1) == 0)
        def _():
            dstate[...] = jnp.zeros_like(dstate)

        qv, kv, vv, dov = q_ref[...], k_ref[...], v_ref[...], do_ref[...]
        beta, g = gb_ref[0, :, 0:1], gb_ref[0, :, 1:2]
        tm, s, dsp = tm_ref[0, 0], s_ref[0, 0], dstate[...]
        row, col, causal, strict, gcum, decay, kb, kk = _gdr_chunk_terms(kv, beta, g)
        e = jnp.exp(gcum)
        vb, kbe = vv * beta, kb * e
        u = _dot(tm, vb, HIGH)
        w = _dot(tm, kbe, HIGH)
        qk = _dot_nt(qv, kv)
        p = jnp.where(causal, qk * decay, 0.0)
        vn = u - _dot(w, s)
        glast = gcum[c - 1:c, :]
        el = jnp.exp(glast)
        f = jnp.exp(glast - gcum)
        kd = kv * f
        qe = qv * e

        dvn = _dot_tn(p, dov) + _dot(kd, dsp)
        dglast = el[:, 0:1] * jnp.sum(s * dsp, keepdims=True)
        dkd = _dot_nt(vn, dsp)
        dk = dkd * f
        df = jnp.sum(dkd * kv, axis=1, keepdims=True) * f[:, 0:1]
        dglast = dglast + jnp.sum(df, keepdims=True)
        dgc = -df
        dp = jnp.where(causal, _dot_nt(dov, vn), 0.0)
        dqe = _dot_nt(dov, s)
        dq = dqe * e
        de = jnp.sum(dqe * qv, axis=1, keepdims=True)
        dstate[...] = dsp * el + _dot_tn(qe, dov) - _dot_tn(w, dvn)
        dw = -_dot_nt(dvn, s)
        dvb = _dot_tn(tm, dvn, HIGH)
        dkbe = _dot_tn(tm, dw, HIGH)
        da = -jnp.where(strict, _dot_nt(dvb, u) + _dot_nt(dkbe, w), 0.0)
        dkk = da * decay
        dqk = dp * decay
        dd = da * kk + dp * qk
        dq = dq + _dot(dqk, kv)
        dk = dk + _dot_tn(dqk, qv)
        dkb = _dot(dkk, kv) + dkbe * e
        dk = dk + _dot_tn(dkk, kb)
        de = de + jnp.sum(dkbe * kb, axis=1, keepdims=True)
        dk = dk + dkb * beta
        dbeta = jnp.sum(dkb * kv, axis=1, keepdims=True) + jnp.sum(dvb * vv, axis=1, keepdims=True)
        m = dd * decay
        dgc = dgc + jnp.sum(m, axis=1, keepdims=True) - jnp.sum(m.T, axis=1, keepdims=True)
        dgc = dgc + de * e[:, 0:1]
        dgc = dgc + jnp.where(row[:, 0:1] == c - 1, dglast, 0.0)
        dg = _dot((row <= col).astype(F32), jnp.broadcast_to(dgc, (c, c)), HIGHEST)
        dq_ref[...] = dq
        dk_ref[...] = dk
        dv_ref[...] = dvb * beta
        dgb_ref[0, :, 0:1] = dbeta
        dgb_ref[0, :, 1:2] = dg[:, 0:1]

    blk = pl.BlockSpec((c, HEAD_A), lambda h, i: (n - 1 - i, h))
    mat = pl.BlockSpec((1, 1, c, c), lambda h, i: (h, n - 1 - i, 0, 0))
    gsp = pl.BlockSpec((1, c, 2), lambda h, i: (h, n - 1 - i, 0))
    return pl.pallas_call(
        body, name=name, grid=(N_HEADS_A, n),
        in_specs=[blk, blk, blk, gsp, mat, mat, blk], out_specs=[blk, blk, blk, gsp],
        out_shape=[jax.ShapeDtypeStruct((t, WIDTH_A), F32)] * 3 + [jax.ShapeDtypeStruct((N_HEADS_A, t, 2), F32)],
        scratch_shapes=[pltpu.VMEM((HEAD_A, HEAD_A), F32)],
        compiler_params=_params(("parallel", "arbitrary")))(q, k, v, gb, tm_all, s_all, do)


def _onorm_fwd(o, gate, g, name):
    t = o.shape[0]

    def body(o_ref, gate_ref, g_ref, y_ref):
        ov, gv = o_ref[...], gate_ref[...]
        r = lax.rsqrt(jnp.mean(ov * ov, axis=-1, keepdims=True) + RMS_EPS)
        y_ref[...] = (ov * r * g_ref[...] * gv * _sigmoid(gv)).astype(BF16)

    blk = pl.BlockSpec((t, HEAD_A), lambda j: (0, j))
    return pl.pallas_call(
        body, name=name, grid=(N_HEADS_A,), in_specs=[blk, blk, pl.BlockSpec((1, HEAD_A), lambda j: (0, 0))],
        out_specs=blk, out_shape=jax.ShapeDtypeStruct((t, WIDTH_A), BF16),
        compiler_params=_params(("parallel",)))(o, gate, g)


def _onorm_bwd(o, gate, g, dy, name):
    t = o.shape[0]

    def body(o_ref, gate_ref, g_ref, dy_ref, do_ref, dgate_ref, dg_ref):
        @pl.when(pl.program_id(0) == 0)
        def _():
            dg_ref[...] = jnp.zeros_like(dg_ref)

        ov, gv, dyv = o_ref[...], gate_ref[...], dy_ref[...].astype(F32)
        r = lax.rsqrt(jnp.mean(ov * ov, axis=-1, keepdims=True) + RMS_EPS)
        oh = ov * r
        sg, dsg = _silu_and_grad(gv)
        dgate_ref[...] = (dyv * oh * g_ref[...] * dsg).astype(BF16)
        dn = dyv * sg
        dg_ref[...] += jnp.sum(dn * oh, axis=0, keepdims=True)
        dng = dn * g_ref[...]
        do_ref[...] = r * (dng - oh * jnp.mean(dng * oh, axis=-1, keepdims=True))

    blk = pl.BlockSpec((t, HEAD_A), lambda j: (0, j))
    vec = pl.BlockSpec((1, HEAD_A), lambda j: (0, 0))
    return pl.pallas_call(
        body, name=name, grid=(N_HEADS_A,), in_specs=[blk, blk, vec, blk], out_specs=[blk, blk, vec],
        out_shape=[jax.ShapeDtypeStruct((t, WIDTH_A), F32), jax.ShapeDtypeStruct((t, WIDTH_A), BF16),
                   jax.ShapeDtypeStruct((1, HEAD_A), F32)],
        compiler_params=_params(("arbitrary",)))(o, gate, g, dy)


def _cmul(ar, ai, br, bi):
    return ar * br - ai * bi, ar * bi + ai * br


def _scan_tables(ar, ai, reverse):
    p1 = (ar, ai)
    p2 = _cmul(*p1, *p1)
    p4 = _cmul(*p2, *p2)
    p8 = _cmul(*p4, *p4)
    p3 = _cmul(*p2, *p1)
    p5 = _cmul(*p4, *p1)
    p6 = _cmul(*p4, *p2)
    p7 = _cmul(*p4, *p3)
    pows = [p1, p2, p3, p4, p5, p6, p7, p8]
    rows = lax.broadcasted_iota(jnp.int32, (8, ar.shape[1]), 0)
    tr = jnp.zeros((8, ar.shape[1]), F32)
    ti = jnp.zeros((8, ar.shape[1]), F32)
    for r in range(8):
        pw = pows[7 - r] if reverse else pows[r]
        tr = jnp.where(rows == r, pw[0], tr)
        ti = jnp.where(rows == r, pw[1], ti)
    return p1, p2, p4, p8, tr, ti


def _tile_scan(xr, xi, p1, p2, p4, reverse):
    rows = lax.broadcasted_iota(jnp.int32, xr.shape, 0)
    for s, (pr, pi) in ((1, p1), (2, p2), (4, p4)):
        if reverse:
            keep = rows < 8 - s
            sr, si = pltpu.roll(xr, 8 - s, 0), pltpu.roll(xi, 8 - s, 0)
        else:
            keep = rows >= s
            sr, si = pltpu.roll(xr, s, 0), pltpu.roll(xi, s, 0)
        sr, si = jnp.where(keep, sr, 0.0), jnp.where(keep, si, 0.0)
        mr, mi = _cmul(pr, pi, sr, si)
        xr, xi = xr + mr, xi + mi
    return xr, xi


def _s5_scan_fwd(bu, a, name, tb=512):
    t = bu.shape[0]
    cb = SCAN_CB
    nt = t // tb

    def body(b_ref, a_ref, x_ref, carry):
        @pl.when(pl.program_id(1) == 0)
        def _():
            carry[...] = jnp.zeros_like(carry)

        ar, ai = a_ref[:, 0:cb], a_ref[:, cb:2 * cb]
        p1, p2, p4, p8, tr, ti = _scan_tables(ar, ai, False)

        def step(j, c):
            cr, ci = c
            i = pl.multiple_of(j * 8, 8)
            xr, xi = _tile_scan(b_ref[pl.ds(i, 8), 0:cb], b_ref[pl.ds(i, 8), cb:2 * cb], p1, p2, p4, False)
            mr, mi = _cmul(tr, ti, cr, ci)
            xr, xi = xr + mr, xi + mi
            x_ref[pl.ds(i, 8), 0:cb] = xr
            x_ref[pl.ds(i, 8), cb:2 * cb] = xi
            return xr[7:8, :], xi[7:8, :]

        cr, ci = lax.fori_loop(0, tb // 8, step, (carry[0:1, :], carry[1:2, :]), unroll=2)
        carry[0:1, :] = cr
        carry[1:2, :] = ci

    blk = pl.BlockSpec((tb, 2 * cb), lambda j, i: (i, j))
    return pl.pallas_call(
        body, name=name, grid=(SSM_CH // cb, nt),
        in_specs=[blk, pl.BlockSpec((1, 2 * cb), lambda j, i: (0, j))], out_specs=blk,
        out_shape=jax.ShapeDtypeStruct((t, 2 * SSM_CH), F32), scratch_shapes=[pltpu.VMEM((8, cb), F32)],
        compiler_params=_params(("parallel", "arbitrary")))(bu, a)


def _s5_scan_bwd(dx, x, a, name, tb=512):
    t = dx.shape[0]
    cb = SCAN_CB
    nt = t // tb
    nj = tb // 8

    def body(d_ref, x_ref, xp_ref, a_ref, l_ref, da_ref, carry, acc):
        tblk = pl.program_id(1)

        @pl.when(tblk == 0)
        def _():
            carry[...] = jnp.zeros_like(carry)
            acc[...] = jnp.zeros_like(acc)

        ar, ai = a_ref[:, 0:cb], a_ref[:, cb:2 * cb]
        p1, p2, p4, p8, tr, ti = _scan_tables(ar, -ai, True)
        rows = lax.broadcasted_iota(jnp.int32, (8, cb), 0)

        def step(jj, c):
            cr, ci, sr_acc, si_acc = c
            j = nj - 1 - jj
            i = pl.multiple_of(j * 8, 8)
            lr, li = _tile_scan(d_ref[pl.ds(i, 8), 0:cb], d_ref[pl.ds(i, 8), cb:2 * cb], p1, p2, p4, True)
            mr, mi = _cmul(tr, ti, cr, ci)
            lr, li = lr + mr, li + mi
            l_ref[pl.ds(i, 8), 0:cb] = lr
            l_ref[pl.ds(i, 8), cb:2 * cb] = li
            ip = pl.multiple_of(jnp.maximum(j - 1, 0) * 8, 8)
            prev_r = jnp.where(j > 0, x_ref[pl.ds(ip, 8), 0:cb], xp_ref[:, 0:cb])
            prev_i = jnp.where(j > 0, x_ref[pl.ds(ip, 8), cb:2 * cb], xp_ref[:, cb:2 * cb])
            edge = jnp.where(jnp.logical_and(j == 0, tblk == nt - 1), 0.0, 1.0)
            xs_r = jnp.where(rows == 0, pltpu.roll(prev_r, 1, 0) * edge, pltpu.roll(x_ref[pl.ds(i, 8), 0:cb], 1, 0))
            xs_i = jnp.where(rows == 0, pltpu.roll(prev_i, 1, 0) * edge, pltpu.roll(x_ref[pl.ds(i, 8), cb:2 * cb], 1, 0))
            sr_acc = sr_acc + lr * xs_r + li * xs_i
            si_acc = si_acc + li * xs_r - lr * xs_i
            return lr[0:1, :], li[0:1, :], sr_acc, si_acc

        cr, ci, sr_acc, si_acc = lax.fori_loop(
            0, nj, step, (carry[0:1, :], carry[1:2, :], acc[:, 0:cb], acc[:, cb:2 * cb]))
        carry[0:1, :] = cr
        carry[1:2, :] = ci
        acc[:, 0:cb] = sr_acc
        acc[:, cb:2 * cb] = si_acc

        @pl.when(tblk == nt - 1)
        def _():
            da_ref[...] = jnp.sum(acc[...], axis=0, keepdims=True)

    blk = pl.BlockSpec((tb, 2 * cb), lambda j, i: (nt - 1 - i, j))
    prev = pl.BlockSpec((8, 2 * cb), lambda j, i: (jnp.maximum((nt - 1 - i) * (tb // 8) - 1, 0), j))
    vec = pl.BlockSpec((1, 2 * cb), lambda j, i: (0, j))
    return pl.pallas_call(
        body, name=name, grid=(SSM_CH // cb, nt), in_specs=[blk, blk, prev, vec], out_specs=[blk, vec],
        out_shape=[jax.ShapeDtypeStruct((t, 2 * SSM_CH), F32), jax.ShapeDtypeStruct((1, 2 * SSM_CH), F32)],
        scratch_shapes=[pltpu.VMEM((8, cb), F32), pltpu.VMEM((8, 2 * cb), F32)],
        compiler_params=_params(("parallel", "arbitrary")))(dx, x, x, a)


def _glu_fwd(yc, u, dvec, wg, bg, name, tr=256):
    t = yc.shape[0]

    def body(yc_ref, u_ref, d_ref, w_ref, b_ref, yl_ref, yb_ref):
        yl = yc_ref[...] + d_ref[...] * u_ref[...]
        yl_ref[...] = yl
        yg, _ = _gelu_and_grad(yl)
        z = jnp.dot(yg.astype(BF16), w_ref[...], preferred_element_type=F32) + b_ref[...]
        yb_ref[...] = (yg * _sigmoid(z)).astype(BF16)

    blk = pl.BlockSpec((tr, SSM_WIDTH), lambda i: (i, 0))
    vec = pl.BlockSpec((1, SSM_WIDTH), lambda i: (0, 0))
    return pl.pallas_call(
        body, name=name, grid=(t // tr,),
        in_specs=[blk, blk, vec, pl.BlockSpec((SSM_WIDTH, SSM_WIDTH), lambda i: (0, 0)), vec],
        out_specs=[blk, blk],
        out_shape=[jax.ShapeDtypeStruct((t, SSM_WIDTH), F32), jax.ShapeDtypeStruct((t, SSM_WIDTH), BF16)],
        compiler_params=_params(("parallel",)))(yc, u, dvec, wg, bg)


def _glu_bwd(yl, u, dvec, wg, bg, dyb, name, tr=256):
    t = yl.shape[0]

    def body(yl_ref, u_ref, d_ref, w_ref, b_ref, dy_ref, dyl_ref, du_ref, dw_ref, db_ref, dd_ref):
        @pl.when(pl.program_id(0) == 0)
        def _():
            dw_ref[...] = jnp.zeros_like(dw_ref)
            db_ref[...] = jnp.zeros_like(db_ref)
            dd_ref[...] = jnp.zeros_like(dd_ref)

        ylv, dyv, wv = yl_ref[...], dy_ref[...].astype(F32), w_ref[...]
        yg, dgelu = _gelu_and_grad(ylv)
        ygb = yg.astype(BF16)
        z = jnp.dot(ygb, wv, preferred_element_type=F32) + b_ref[...]
        sg = _sigmoid(z)
        dz = dyv * yg * sg * (1.0 - sg)
        dzb = dz.astype(BF16)
        dyg = dyv * sg + lax.dot_general(dzb, wv, (((1,), (1,)), ((), ())), preferred_element_type=F32)
        dyl = dyg * dgelu
        dyl_ref[...] = dyl.astype(BF16)
        du_ref[...] = dyl * d_ref[...]
        dw_ref[...] += lax.dot_general(ygb, dzb, (((0,), (0,)), ((), ())), preferred_element_type=F32)
        db_ref[...] += jnp.sum(dz, axis=0, keepdims=True)
        dd_ref[...] += jnp.sum(dyl * u_ref[...], axis=0, keepdims=True)

    blk = pl.BlockSpec((tr, SSM_WIDTH), lambda i: (i, 0))
    vec = pl.BlockSpec((1, SSM_WIDTH), lambda i: (0, 0))
    wsp = pl.BlockSpec((SSM_WIDTH, SSM_WIDTH), lambda i: (0, 0))
    return pl.pallas_call(
        body, name=name, grid=(t // tr,), in_specs=[blk, blk, vec, wsp, vec, blk],
        out_specs=[blk, blk, wsp, vec, vec],
        out_shape=[jax.ShapeDtypeStruct((t, SSM_WIDTH), BF16), jax.ShapeDtypeStruct((t, SSM_WIDTH), F32),
                   jax.ShapeDtypeStruct((SSM_WIDTH, SSM_WIDTH), F32), jax.ShapeDtypeStruct((1, SSM_WIDTH), F32),
                   jax.ShapeDtypeStruct((1, SSM_WIDTH), F32)],
        compiler_params=_params(("arbitrary",)))(yl, u, dvec, wg, bg, dyb)


def _mesh_pos():
    return lax.axis_index("x"), lax.axis_index("y"), lax.axis_index("c")


def _all_gather(arrays, name):
    na = len(arrays)

    def body(*refs):
        ins, outs = refs[:na], refs[na:2 * na]
        send_sems, recv_sems, local_sems = refs[2 * na:]
        x, y, c = _mesh_pos()
        me, sibling = (x, y, c), (x, y, 1 - c)
        chips = [(1 - x, y), (x, 1 - y), (1 - x, 1 - y)]
        waits = []
        for ai in range(na):
            in_ref, out_ref = ins[ai], outs[ai]

            def slot(px, py, pc, out_ref=out_ref):
                return out_ref.at[4 * px + 2 * py + pc]

            def copy(kk, block, to, src=None, ai=ai, slot=slot):
                return pltpu.make_async_remote_copy(
                    src_ref=slot(*block) if src is None else src, dst_ref=slot(*block),
                    send_sem=send_sems.at[ai, kk], recv_sem=recv_sems.at[ai, kk], device_id=to, device_id_type=MESH)

            mine = pltpu.make_async_copy(in_ref, slot(*me), local_sems.at[ai])
            mine.start()
            first = [copy(0, me, sibling, src=in_ref)]
            first += [copy(1 + j, me, (*chip, c), src=in_ref) for j, chip in enumerate(chips)]
            for cp in first:
                cp.start()
            waits.append((copy, mine, first))
        sends = []
        for ai in range(na):
            copy, mine, first = waits[ai]
            passed = [copy(4 + j, (*chip, c), sibling) for j, chip in enumerate(chips)]
            for j, chip in enumerate(chips):
                copy(1 + j, (*chip, c), me).wait_recv()
                passed[j].start()
            sends.append(passed)
        for ai in range(na):
            copy, mine, first = waits[ai]
            copy(0, sibling, me).wait_recv()
            for j, chip in enumerate(chips):
                copy(4 + j, (*chip, 1 - c), me).wait_recv()
            for cp in first + sends[ai]:
                cp.wait_send()
            mine.wait()

    any_spec = pl.BlockSpec(memory_space=pl.ANY)
    return pl.pallas_call(
        body, name=name, in_specs=[any_spec] * na, out_specs=[any_spec] * na,
        out_shape=[jax.ShapeDtypeStruct((N_DEV,) + a.shape, a.dtype) for a in arrays],
        scratch_shapes=[pltpu.SemaphoreType.DMA((na, 7)), pltpu.SemaphoreType.DMA((na, 7)),
                        pltpu.SemaphoreType.DMA((na,))],
        compiler_params=pltpu.CompilerParams(has_side_effects=True))(*arrays)


def _swap_sibling(send, name):
    def body(s_ref, o_ref, send_sem, recv_sem):
        x, y, c = _mesh_pos()
        cp = pltpu.make_async_remote_copy(src_ref=s_ref.at[1 - c], dst_ref=o_ref, send_sem=send_sem, recv_sem=recv_sem,
                                          device_id=(x, y, 1 - c), device_id_type=MESH)
        cp.start()
        cp.wait()

    any_spec = pl.BlockSpec(memory_space=pl.ANY)
    return pl.pallas_call(
        body, name=name, in_specs=[any_spec], out_specs=any_spec,
        out_shape=jax.ShapeDtypeStruct(send.shape[1:], send.dtype),
        scratch_shapes=[pltpu.SemaphoreType.DMA, pltpu.SemaphoreType.DMA],
        compiler_params=pltpu.CompilerParams(has_side_effects=True))(send)


def _exchange_chips(send, name):
    def body(s_ref, o_ref, send_sems, recv_sems, local_sem):
        x, y, c = _mesh_pos()
        my_chip = 2 * x + y
        chips = [(1 - x, y), (x, 1 - y), (1 - x, 1 - y)]
        mine = pltpu.make_async_copy(s_ref.at[my_chip], o_ref.at[my_chip], local_sem)
        mine.start()
        cps = [pltpu.make_async_remote_copy(
            src_ref=s_ref.at[2 * cx + cy], dst_ref=o_ref.at[my_chip], send_sem=send_sems.at[j], recv_sem=recv_sems.at[j],
            device_id=(cx, cy, c), device_id_type=MESH) for j, (cx, cy) in enumerate(chips)]
        for cp in cps:
            cp.start()
        for j, (cx, cy) in enumerate(chips):
            pltpu.make_async_remote_copy(
                src_ref=s_ref.at[my_chip], dst_ref=o_ref.at[2 * cx + cy], send_sem=send_sems.at[j],
                recv_sem=recv_sems.at[j], device_id=(cx, cy, c), device_id_type=MESH).wait_recv()
        for cp in cps:
            cp.wait_send()
        mine.wait()

    any_spec = pl.BlockSpec(memory_space=pl.ANY)
    return pl.pallas_call(
        body, name=name, in_specs=[any_spec], out_specs=any_spec,
        out_shape=jax.ShapeDtypeStruct(send.shape, send.dtype),
        scratch_shapes=[pltpu.SemaphoreType.DMA((3,)), pltpu.SemaphoreType.DMA((3,)), pltpu.SemaphoreType.DMA],
        compiler_params=pltpu.CompilerParams(has_side_effects=True))(send)


def _pair_sum(grads, recv, name, tr=464):
    _, nchip, rows, cols = grads.shape
    c_idx = lax.axis_index("c").astype(jnp.int32).reshape(1)

    def body(c_ref, g_ref, r_ref, o_ref):
        del c_ref
        o_ref[0] = (g_ref[0, 0].astype(F32) + r_ref[0].astype(F32)).astype(BF16)

    grid_spec = pltpu.PrefetchScalarGridSpec(
        num_scalar_prefetch=1, grid=(nchip, rows // tr),
        in_specs=[pl.BlockSpec((1, 1, tr, cols), lambda k, i, c_ref: (c_ref[0], k, i, 0)),
                  pl.BlockSpec((1, tr, cols), lambda k, i, c_ref: (k, i, 0))],
        out_specs=pl.BlockSpec((1, tr, cols), lambda k, i, c_ref: (k, i, 0)))
    return pl.pallas_call(
        body, name=name, grid_spec=grid_spec, out_shape=jax.ShapeDtypeStruct((nchip, rows, cols), BF16),
        compiler_params=_params(("parallel", "parallel")))(c_idx, grads, recv)


def _sum_leading(parts, name, tr=464):
    nparts, rows, cols = parts.shape
    tr = tr if rows % tr == 0 else rows

    def body(p_ref, o_ref):
        acc = p_ref[0].astype(F32)
        for i in range(1, nparts):
            acc = acc + p_ref[i].astype(F32)
        o_ref[...] = acc

    return pl.pallas_call(
        body, name=name, grid=(rows // tr,),
        in_specs=[pl.BlockSpec((nparts, tr, cols), lambda i: (0, i, 0))],
        out_specs=pl.BlockSpec((tr, cols), lambda i: (i, 0)), out_shape=jax.ShapeDtypeStruct((rows, cols), F32),
        compiler_params=_params(("parallel",)))(parts)


def _adamw(w, g, m, v, name):
    shape = w.shape
    cols = shape[-1]
    rows = int(np.prod(shape[:-1])) if len(shape) > 1 else 1
    w2, g2, m2, v2 = (a.reshape(rows, cols) for a in (w, g, m, v))
    tr = rows
    for cand in (512, 256, 128, 64, 32, 16, 8):
        if rows % cand == 0 and rows > cand:
            tr = cand
            break
    bc1, bc2 = 1.0 - ADAM_B1 ** ADAM_STEP, 1.0 - ADAM_B2 ** ADAM_STEP

    def body(w_ref, g_ref, m_ref, v_ref, d_ref, nm_ref, nv_ref):
        gv = g_ref[...]
        nm = ADAM_B1 * m_ref[...] + (1.0 - ADAM_B1) * gv
        nv = ADAM_B2 * v_ref[...] + (1.0 - ADAM_B2) * (gv * gv)
        nm_ref[...] = nm
        nv_ref[...] = nv
        d_ref[...] = -ADAM_LR * ((nm / bc1) / (jnp.sqrt(nv / bc2) + ADAM_EPS) + ADAM_WD * w_ref[...])

    blk = pl.BlockSpec((tr, cols), lambda i: (i, 0))
    outs = pl.pallas_call(
        body, name=name, grid=(rows // tr,), in_specs=[blk] * 4, out_specs=[blk] * 3,
        out_shape=[jax.ShapeDtypeStruct((rows, cols), F32)] * 3, compiler_params=_params(("parallel",)))(w2, g2, m2, v2)
    return tuple(o.reshape(shape) for o in outs)


WEIGHT_NAMES = ['norm_mix_g', 'norm_xa_g', 'norm_ffn_g', 'norm_mem_g', 'norm_final_g', 'w_in_ab', 'conv_qkv_a',
                'a_log_a', 'dt_bias_a', 'onorm_g_a', 'ssm_lambda_re', 'ssm_lambda_im', 'ssm_b_re', 'ssm_b_im',
                'ssm_c_re', 'ssm_c_im', 'ssm_d', 'ssm_log_dt', 'w_glu_b', 'b_glu_b', 'w_out_ab', 'pool_w',
                'pool_scale', 'xa_wq', 'xa_wkv', 'xa_wo', 'ffn_w_up', 'ffn_conv', 'ffn_w_down']
BIG_SHARDED = {'w_in_ab': ((1, 1024, 2568), 2), 'w_glu_b': ((1, 512, 512), 1), 'w_out_ab': ((1, 1024, 1024), 1),
               'pool_w': ((1, 4, 256, 256), 2), 'xa_wq': ((2, 1024, 1024), 1), 'xa_wkv': ((2, 1024, 2048), 2),
               'xa_wo': ((2, 1024, 1024), 1), 'ffn_w_up': ((2, 1024, 5632), 2), 'ffn_w_down': ((2, 2816, 1024), 1)}
SMALL_SHARDED = {'conv_qkv_a': ((1, 4, 1536), 2), 'pool_scale': ((1, 1024), 1), 'ffn_conv': ((2, 3, 5632), 2)}
REPLICATED = {'norm_mix_g': (2, 1024), 'norm_xa_g': (2, 1024), 'norm_ffn_g': (2, 1024), 'norm_mem_g': (1024,),
              'norm_final_g': (1024,), 'a_log_a': (1, 4), 'dt_bias_a': (1, 4), 'onorm_g_a': (1, 128),
              'ssm_lambda_re': (1, 32, 64), 'ssm_lambda_im': (1, 32, 64), 'ssm_b_re': (1, 32, 64, 16),
              'ssm_b_im': (1, 32, 64, 16), 'ssm_c_re': (1, 32, 16, 64), 'ssm_c_im': (1, 32, 16, 64),
              'ssm_d': (1, 32, 16), 'ssm_log_dt': (1, 32), 'b_glu_b': (1, 512)}
BIG_ROW_ALIGN = 464 * 8


def _shard_shape(shape, axis):
    return tuple(s // N_DEV if i == axis else s for i, s in enumerate(shape))


def _pad_rows(a, rows):
    return a if a.shape[0] == rows else jnp.concatenate([a, jnp.zeros((rows - a.shape[0],) + a.shape[1:], a.dtype)], 0)


def _round_up(n, m):
    return (n + m - 1) // m * m


def _pack_rows(flat_list, align):
    rows = []
    for a in flat_list:
        n = a.shape[0]
        r = _round_up(n, PACK_COLS) // PACK_COLS
        if n != r * PACK_COLS:
            a = jnp.concatenate([a, jnp.zeros((r * PACK_COLS - n,), a.dtype)])
        rows.append(a.reshape(r, PACK_COLS))
    out = jnp.concatenate(rows, 0)
    return _pad_rows(out, _round_up(out.shape[0], align))


def _row_offsets(sizes):
    offs, r = [], 0
    for n in sizes:
        offs.append(r)
        r += _round_up(n, PACK_COLS) // PACK_COLS
    return offs


def _split_shards(full, axis):
    shape = full.shape
    s = shape[axis] // N_DEV
    a = full.reshape(shape[:axis] + (N_DEV, s) + shape[axis + 1:])
    return jnp.moveaxis(a, axis, 0).reshape(N_DEV, -1)


def _merge_shards(pieces, shape, axis):
    sh = _shard_shape(shape, axis)
    a = pieces.reshape((N_DEV,) + sh)
    a = jnp.moveaxis(a, 0, axis)
    return a.reshape(shape)


def _scan_perm():
    perm = np.zeros((2 * SSM_CH,), np.int32)
    for col in range(2 * SSM_CH):
        blk, rem = divmod(col, 2 * SCAN_CB)
        part, cc = divmod(rem, SCAN_CB)
        perm[col] = part * SSM_CH + blk * SCAN_CB + cc
    return perm


_SCAN_PERM = _scan_perm()
_SCAN_INV = np.argsort(_SCAN_PERM).astype(np.int32)


def _s5_discretise(lam_re, lam_im, b_re, b_im, log_dt):
    dt = jnp.exp(log_dt)[:, None]
    mag = jnp.exp(lam_re * dt)
    ang = lam_im * dt
    lb_re, lb_im = mag * jnp.cos(ang), mag * jnp.sin(ang)
    den = lam_re * lam_re + lam_im * lam_im
    nr, ni = lb_re - 1.0, lb_im
    coef_re = (nr * lam_re + ni * lam_im) / den
    coef_im = (ni * lam_re - nr * lam_im) / den
    bb_re = coef_re[..., None] * b_re - coef_im[..., None] * b_im
    bb_im = coef_re[..., None] * b_im + coef_im[..., None] * b_re
    return lb_re, lb_im, bb_re, bb_im


def _s5_matrices(lb_re, lb_im, bb_re, bb_im, c_re, c_im):
    eye = jnp.eye(N_GROUPS, dtype=F32)
    bmat = lambda bb: jnp.einsum('gph,gk->ghkp', bb, eye).reshape(SSM_WIDTH, SSM_CH)
    cmat = lambda cc: jnp.einsum('ghp,gk->gpkh', cc, eye).reshape(SSM_CH, SSM_WIDTH)
    b_in = jnp.concatenate([bmat(bb_re), bmat(bb_im)], axis=1)[:, _SCAN_PERM]
    c_out = jnp.concatenate([cmat(c_re), -cmat(c_im)], axis=0)[_SCAN_PERM, :]
    a_row = jnp.concatenate([lb_re.reshape(1, SSM_CH), lb_im.reshape(1, SSM_CH)], axis=1)[:, _SCAN_PERM]
    return b_in, c_out, a_row


def _s5_matrix_grads(db_in, dc_out, da_row):
    db_nat = db_in[:, _SCAN_INV]
    dc_nat = dc_out[_SCAN_INV, :]
    da_nat = da_row[:, _SCAN_INV]
    eye = jnp.eye(N_GROUPS, dtype=F32)
    bgrad = lambda m: jnp.einsum('ghkp,gk->gph', m.reshape(N_GROUPS, SSM_GROUP, N_GROUPS, SSM_STATE), eye)
    cgrad = lambda m: jnp.einsum('gpkh,gk->ghp', m.reshape(N_GROUPS, SSM_STATE, N_GROUPS, SSM_GROUP), eye)
    dbb_re, dbb_im = bgrad(db_nat[:, :SSM_CH]), bgrad(db_nat[:, SSM_CH:])
    dc_re, dc_im = cgrad(dc_nat[:SSM_CH]), -cgrad(dc_nat[SSM_CH:])
    dlb_re = da_nat[0, :SSM_CH].reshape(N_GROUPS, SSM_STATE)
    dlb_im = da_nat[0, SSM_CH:].reshape(N_GROUPS, SSM_STATE)
    return dlb_re, dlb_im, dbb_re, dbb_im, dc_re, dc_im


def _hybrid_fwd(xn, x, wts, p):
    sv = {}
    hq = _mm(xn, wts['w_qkv'], "nn", "l0_in_qkv")
    gate = _mm(xn, wts['w_gate'], "nn", "l0_in_gate")
    ba = _mm(xn, wts['w_ba'], "nn", "l0_in_ba")
    u = _mm(xn, wts['w_u'], "nn", "l0_in_u")
    conv = p['conv_qkv']
    q = _qkv_pre_fwd(hq, conv, 0, 4, True, HEAD_A ** -0.5, "l0_q_pre")
    k = _qkv_pre_fwd(hq, conv, 4, 4, True, 1.0, "l0_k_pre")
    v = _qkv_pre_fwd(hq, conv, 8, 4, False, 1.0, "l0_v_pre")
    gates = _gates_fwd(ba, p['arow'], p['brow'], "l0_gates")
    gb = jnp.stack([gates[:, 0:4].T, gates[:, 4:8].T], axis=-1)
    o, tm_all, s_all = _gdr_fwd(q, k, v, gb, "l0_gdr_fwd")
    y_a = _onorm_fwd(o, gate, p['onorm_g'], "l0_onorm")
    bu = _mm(u, p['b_in'], "nn", "l0_s5_bu")
    xs = _s5_scan_fwd(bu, p['a_row'], "l0_s5_scan")
    yc = _mm(xs, p['c_out'], "nn", "l0_s5_cx")
    yl, y_b = _glu_fwd(yc, u, p['d_row'], wts['w_glu'], p['b_glu'], "l0_glu")
    mixed = jnp.concatenate([y_a, y_b], axis=1)
    x1 = _mm(mixed, wts['w_out'], "nn", "l0_out", res=x)
    sv.update(hq=hq, gate=gate, ba=ba, u=u, q=q, k=k, v=v, gb=gb, o=o, tm=tm_all, s=s_all, xs=xs, yl=yl, mixed=mixed)
    return x1, sv


def _hybrid_bwd(dx1, xn, wts, p, sv):
    gr = {}
    dmixed = _mm(dx1, wts['w_out'], "nt", "l0_out_dx", out_dtype=BF16)
    gr['w_out_ab'] = _mm(sv['mixed'], dx1, "tn", "l0_out_dw", out_dtype=BF16)
    dya, dyb = dmixed[:, :WIDTH_A], dmixed[:, WIDTH_A:]
    dyl, du_direct, gr['w_glu_b'], gr['b_glu_b'], dd = _glu_bwd(
        sv['yl'], sv['u'], p['d_row'], wts['w_glu'], p['b_glu'], dyb, "l0_glu_bwd")
    dxs = _mm(dyl, p['c_out'], "nt", "l0_s5_cx_dx")
    dc_out = _mm(sv['xs'], dyl, "tn", "l0_s5_cx_dw")
    lam, da_row = _s5_scan_bwd(dxs, sv['xs'], p['a_row'], "l0_s5_scan_bwd")
    du = _mm(lam, p['b_in'], "nt", "l0_s5_bu_dx", res=du_direct, out_dtype=BF16)
    db_in = _mm(sv['u'], lam, "tn", "l0_s5_bu_dw")
    gr['s5'] = (db_in, dc_out, da_row, dd)
    do, dgate, gr['onorm_g_a'] = _onorm_bwd(sv['o'], sv['gate'], p['onorm_g'], dya, "l0_onorm_bwd")
    dq, dk, dv, dgb = _gdr_bwd(sv['q'], sv['k'], sv['v'], sv['gb'], sv['tm'], sv['s'], do, "l0_gdr_bwd")
    conv = p['conv_qkv']
    dhq_q, dcw_q = _qkv_pre_bwd(sv['hq'], conv, dq, 0, 4, True, HEAD_A ** -0.5, "l0_q_pre_bwd")
    dhq_k, dcw_k = _qkv_pre_bwd(sv['hq'], conv, dk, 4, 4, True, 1.0, "l0_k_pre_bwd")
    dhq_v, dcw_v = _qkv_pre_bwd(sv['hq'], conv, dv, 8, 4, False, 1.0, "l0_v_pre_bwd")
    gr['conv_qkv_a'] = jnp.concatenate([dcw_q, dcw_k, dcw_v], axis=1)
    dhq = jnp.concatenate([dhq_q, dhq_k, dhq_v], axis=1)
    dgates = jnp.concatenate([dgb[:, :, 0].T, dgb[:, :, 1].T, jnp.zeros((SEQ, LANE - 8), F32)], axis=1)
    dba, da_log, ddt_bias = _gates_bwd(sv['ba'], p['arow'], p['brow'], dgates, "l0_gates_bwd")
    gr['a_log_a'], gr['dt_bias_a'] = da_log[:, 4:8], ddt_bias[:, 4:8]
    dxn = _mm(dhq, wts['w_qkv'], "nt", "l0_in_qkv_dx")
    dxn = _mm(dgate, wts['w_gate'], "nt", "l0_in_gate_dx", res=dxn)
    dxn = _mm(dba, wts['w_ba'], "nt", "l0_in_ba_dx", res=dxn)
    dxn = _mm(du, wts['w_u'], "nt", "l0_in_u_dx", res=dxn)
    dw_qkv = _mm(xn, dhq, "tn", "l0_in_qkv_dw", out_dtype=BF16)
    dw_gate = _mm(xn, dgate, "tn", "l0_in_gate_dw", out_dtype=BF16)
    dw_ba = _mm(xn, dba, "tn", "l0_in_ba_dw", out_dtype=BF16)
    dw_u = _mm(xn, du, "tn", "l0_in_u_dw", out_dtype=BF16)
    gr['w_in_ab'] = jnp.concatenate([dw_qkv, dw_gate, dw_ba[:, :8], dw_u], axis=1)
    return dxn, gr


def _xa_fwd(x1, g, mem_n, wq, wkv, wo, tag):
    xq = _rms_fwd(x1, g, BF16, tag + "_norm")
    q = _mm(xq, wq, "nn", tag + "_q", out_dtype=BF16)
    kv = _mm(mem_n, wkv, "nn", tag + "_kv", out_dtype=BF16)
    o = _attn_fwd(q, kv, tag + "_attn")
    x2 = _mm(o, wo, "nn", tag + "_o", res=x1)
    return x2, dict(xq=xq, q=q, kv=kv, o=o)


def _xa_bwd(dx2, x1, g, mem_n, wq, wkv, wo, sv, tag):
    do = _mm(dx2, wo, "nt", tag + "_o_dx", out_dtype=BF16)
    dwo = _mm(sv['o'], dx2, "tn", tag + "_o_dw", out_dtype=BF16)
    dq, dk, dv = _attn_bwd(sv['q'], sv['kv'], do, tag + "_attn_bwd")
    dkv = jnp.concatenate([dk, dv], axis=1).astype(BF16)
    dxq = _mm(dq, wq, "nt", tag + "_q_dx")
    dwq = _mm(sv['xq'], dq, "tn", tag + "_q_dw", out_dtype=BF16)
    dmem_n = _mm(dkv, wkv, "nt", tag + "_kv_dx")
    dwkv = _mm(mem_n, dkv, "tn", tag + "_kv_dw", out_dtype=BF16)
    dx1, dg = _rms_bwd(x1, g, dxq, dx2, tag + "_norm_bwd")
    return dx1, dmem_n, dict(wq=dwq, wkv=dwkv, wo=dwo, g=dg)


def _ffn_fwd(x2, g, w_up, conv, w_down, tag):
    xf = _rms_fwd(x2, g, BF16, tag + "_norm")
    h = _mm(xf, w_up, "nn", tag + "_up")
    a = _ffn_act_fwd(h, conv, tag + "_act")
    x3 = _mm(a, w_down, "nn", tag + "_down", res=x2)
    return x3, dict(xf=xf, h=h, a=a)


def _ffn_bwd(dx3, x2, g, w_up, conv, w_down, sv, tag):
    da = _mm(dx3, w_down, "nt", tag + "_down_dx")
    dw_down = _mm(sv['a'], dx3, "tn", tag + "_down_dw", out_dtype=BF16)
    dh, dconv = _ffn_act_bwd(sv['h'], conv, da, tag + "_act_bwd")
    dxf = _mm(dh, w_up, "nt", tag + "_up_dx")
    dw_up = _mm(sv['xf'], dh, "tn", tag + "_up_dw", out_dtype=BF16)
    dx2, dg = _rms_bwd(x2, g, dxf, dx3, tag + "_norm_bwd")
    return dx2, dict(w_up=dw_up, conv=dconv, w_down=dw_down, g=dg)


BIG_NAMES, SMALL_NAMES, REP_NAMES = list(BIG_SHARDED), list(SMALL_SHARDED), list(REPLICATED)
BIG_SIZES = [int(np.prod(_shard_shape(*BIG_SHARDED[n]))) for n in BIG_NAMES]
SMALL_SIZES = [int(np.prod(_shard_shape(*SMALL_SHARDED[n]))) for n in SMALL_NAMES]


def _gather_weights(inp):
    big_offs, small_offs = _row_offsets(BIG_SIZES), _row_offsets(SMALL_SIZES)
    big_local = _pack_rows([inp[n].astype(BF16).reshape(-1) for n in BIG_NAMES], BIG_ROW_ALIGN)
    small_local = _pack_rows([inp[n].reshape(-1) for n in SMALL_NAMES], 8)
    big_all, small_all = _all_gather([big_local, small_local], "gather_weights")
    full = {}
    for n, off, size in zip(BIG_NAMES, big_offs, BIG_SIZES):
        r = size // PACK_COLS
        full[n] = _merge_shards(big_all[:, off:off + r].reshape(N_DEV, size), *BIG_SHARDED[n])
    for n, off, size in zip(SMALL_NAMES, small_offs, SMALL_SIZES):
        r = _round_up(size, PACK_COLS) // PACK_COLS
        full[n] = _merge_shards(small_all[:, off:off + r].reshape(N_DEV, -1)[:, :size], *SMALL_SHARDED[n])
    return full


def _local_step(inp, full):
    f32_of = lambda n: inp[n].astype(F32)
    w_in = full['w_in_ab'][0]
    wts0 = dict(w_qkv=w_in[:, :3 * WIDTH_A], w_gate=w_in[:, 3 * WIDTH_A:4 * WIDTH_A],
                w_ba=jnp.concatenate([w_in[:, 4 * WIDTH_A:4 * WIDTH_A + 8], jnp.zeros((D_MODEL, LANE - 8), BF16)], 1),
                w_u=w_in[:, 4 * WIDTH_A + 8:], w_glu=full['w_glu_b'][0], w_out=full['w_out_ab'][0])
    lb_disc, disc_vjp = jax.vjp(_s5_discretise, f32_of('ssm_lambda_re')[0], f32_of('ssm_lambda_im')[0],
                                f32_of('ssm_b_re')[0], f32_of('ssm_b_im')[0], f32_of('ssm_log_dt')[0])
    b_in, c_out, a_row = _s5_matrices(*lb_disc, f32_of('ssm_c_re')[0], f32_of('ssm_c_im')[0])
    zeros4 = jnp.zeros((1, 4), F32)
    p0 = dict(conv_qkv=full['conv_qkv_a'][0], onorm_g=f32_of('onorm_g_a'),
              arow=jnp.concatenate([zeros4, f32_of('a_log_a'), jnp.zeros((1, LANE - 8), F32)], 1),
              brow=jnp.concatenate([zeros4, f32_of('dt_bias_a'), jnp.zeros((1, LANE - 8), F32)], 1),
              b_in=b_in.astype(BF16), c_out=c_out.astype(BF16), a_row=a_row,
              d_row=f32_of('ssm_d').reshape(1, SSM_WIDTH), b_glu=f32_of('b_glu_b'))

    x0 = inp['x'][0]
    mem_n = _rms_fwd(inp['mem'][0], inp['norm_mem_g'], BF16, "mem_norm")
    xn0 = _rms_fwd(x0, inp['norm_mix_g'][0], BF16, "l0_mix_norm")
    x1, sv_mix0 = _hybrid_fwd(xn0, x0, wts0, p0)
    x2, sv_xa0 = _xa_fwd(x1, inp['norm_xa_g'][0], mem_n, full['xa_wq'][0], full['xa_wkv'][0], full['xa_wo'][0], "l0_xa")
    x3, sv_ffn0 = _ffn_fwd(x2, inp['norm_ffn_g'][0], full['ffn_w_up'][0], full['ffn_conv'][0], full['ffn_w_down'][0], "l0_ffn")
    xn1 = _rms_fwd(x3, inp['norm_mix_g'][1], F32, "l1_mix_norm")
    x4 = _pool_fwd(xn1, full['pool_w'][0], full['pool_scale'], x3, "l1_pool")
    x5, sv_xa1 = _xa_fwd(x4, inp['norm_xa_g'][1], mem_n, full['xa_wq'][1], full['xa_wkv'][1], full['xa_wo'][1], "l1_xa")
    x6, sv_ffn1 = _ffn_fwd(x5, inp['norm_ffn_g'][1], full['ffn_w_up'][1], full['ffn_conv'][1], full['ffn_w_down'][1], "l1_ffn")
    loss_part, dx6, dg_final = _loss_head(x6, inp['norm_final_g'], inp['loss_target'][0], "loss_head")

    dx5, g_ffn1 = _ffn_bwd(dx6, x5, inp['norm_ffn_g'][1], full['ffn_w_up'][1], full['ffn_conv'][1], full['ffn_w_down'][1], sv_ffn1, "l1_ffn")
    dx4, dmem1, g_xa1 = _xa_bwd(dx5, x4, inp['norm_xa_g'][1], mem_n, full['xa_wq'][1], full['xa_wkv'][1], full['xa_wo'][1], sv_xa1, "l1_xa")
    dxn1, dpool_w, dpool_scale = _pool_bwd(xn1, full['pool_w'][0], full['pool_scale'], dx4, "l1_pool_bwd")
    dx3, dg_mix1 = _rms_bwd(x3, inp['norm_mix_g'][1], dxn1, dx4, "l1_mix_norm_bwd")
    dx2, g_ffn0 = _ffn_bwd(dx3, x2, inp['norm_ffn_g'][0], full['ffn_w_up'][0], full['ffn_conv'][0], full['ffn_w_down'][0], sv_ffn0, "l0_ffn")
    dx1, dmem0, g_xa0 = _xa_bwd(dx2, x1, inp['norm_xa_g'][0], mem_n, full['xa_wq'][0], full['xa_wkv'][0], full['xa_wo'][0], sv_xa0, "l0_xa")
    dxn0, g_mix0 = _hybrid_bwd(dx1, xn0, wts0, p0, sv_mix0)
    grad_x, dg_mix0 = _rms_bwd(x0, inp['norm_mix_g'][0], dxn0, dx1, "l0_mix_norm_bwd")
    _, dg_mem = _rms_bwd(inp['mem'][0], inp['norm_mem_g'], dmem0 + dmem1, None, "mem_norm_bwd")

    db_in, dc_out, da_row, dd = g_mix0['s5']
    dlb_re, dlb_im, dbb_re, dbb_im, dc_re, dc_im = _s5_matrix_grads(db_in, dc_out, da_row)
    dlam_re, dlam_im, dbr, dbi, dlog_dt = disc_vjp((dlb_re, dlb_im, dbb_re, dbb_im))

    big_grads = {
        'w_in_ab': g_mix0['w_in_ab'][None], 'w_glu_b': g_mix0['w_glu_b'][None], 'w_out_ab': g_mix0['w_out_ab'][None],
        'pool_w': dpool_w[None], 'xa_wq': jnp.stack([g_xa0['wq'], g_xa1['wq']]),
        'xa_wkv': jnp.stack([g_xa0['wkv'], g_xa1['wkv']]), 'xa_wo': jnp.stack([g_xa0['wo'], g_xa1['wo']]),
        'ffn_w_up': jnp.stack([g_ffn0['w_up'], g_ffn1['w_up']]), 'ffn_w_down': jnp.stack([g_ffn0['w_down'], g_ffn1['w_down']])}
    rep_grads = {
        'norm_mix_g': jnp.concatenate([dg_mix0, dg_mix1], 0), 'norm_xa_g': jnp.concatenate([g_xa0['g'], g_xa1['g']], 0),
        'norm_ffn_g': jnp.concatenate([g_ffn0['g'], g_ffn1['g']], 0), 'norm_mem_g': dg_mem.reshape(-1),
        'norm_final_g': dg_final.reshape(-1), 'a_log_a': g_mix0['a_log_a'], 'dt_bias_a': g_mix0['dt_bias_a'],
        'onorm_g_a': g_mix0['onorm_g_a'], 'ssm_lambda_re': dlam_re[None], 'ssm_lambda_im': dlam_im[None],
        'ssm_b_re': dbr[None], 'ssm_b_im': dbi[None], 'ssm_c_re': dc_re[None], 'ssm_c_im': dc_im[None],
        'ssm_d': dd.reshape(1, N_GROUPS, SSM_GROUP), 'ssm_log_dt': dlog_dt[None], 'b_glu_b': g_mix0['b_glu_b']}
    small_grads = {'conv_qkv_a': g_mix0['conv_qkv_a'][None], 'pool_scale': dpool_scale,
                   'ffn_conv': jnp.stack([g_ffn0['conv'], g_ffn1['conv']])}
    return loss_part, grad_x, big_grads, rep_grads, small_grads


def _reduce_gradients(big_grads, rep_grads, small_grads):
    pieces = jnp.concatenate([_split_shards(big_grads[n].astype(BF16), BIG_SHARDED[n][1]).reshape(N_DEV, -1, PACK_COLS)
                              for n in BIG_NAMES], axis=1)
    rows_used = pieces.shape[1]
    rows_pad = _round_up(rows_used, BIG_ROW_ALIGN)
    pieces = jnp.concatenate([pieces, jnp.zeros((N_DEV, rows_pad - rows_used, PACK_COLS), BF16)], axis=1)
    by_core = jnp.transpose(pieces.reshape(4, 2, rows_pad, PACK_COLS), (1, 0, 2, 3))
    from_sibling = _swap_sibling(by_core, "grads_to_sibling")
    chip_sums = _pair_sum(by_core, from_sibling, "grads_pair_sum")
    from_chips = _exchange_chips(chip_sums, "grads_to_chips")
    big_reduced = _sum_leading(from_chips, "grads_chip_sum")

    misc_list = [rep_grads[n].astype(F32).reshape(-1) for n in REP_NAMES] + \
                [small_grads[n].astype(F32).reshape(-1) for n in SMALL_NAMES]
    misc_sizes = [int(a.shape[0]) for a in misc_list]
    misc_local = _pack_rows(misc_list, 8)
    (misc_all,) = _all_gather([misc_local], "gather_small_grads")
    misc_sum = _sum_leading(misc_all, "small_grads_sum")
    return big_reduced, misc_sum, misc_sizes


def _update(inp, loss_part, grad_x, big_reduced, misc_sum, misc_sizes):
    big_names, small_names, rep_names = BIG_NAMES, SMALL_NAMES, REP_NAMES
    big_offs, big_sizes, misc_offs = _row_offsets(BIG_SIZES), BIG_SIZES, _row_offsets(misc_sizes)
    dev = 4 * lax.axis_index("x") + 2 * lax.axis_index("y") + lax.axis_index("c")
    grads = {}
    for n, off, size in zip(big_names, big_offs, big_sizes):
        grads[n] = big_reduced[off:off + size // PACK_COLS].reshape(inp[n].shape)
    for n, off, size in zip(rep_names + small_names, misc_offs, misc_sizes):
        flat = misc_sum[off:off + _round_up(size, PACK_COLS) // PACK_COLS].reshape(-1)[:size]
        if n in REPLICATED:
            grads[n] = flat.reshape(inp[n].shape)
        else:
            shape, axis = SMALL_SHARDED[n]
            grads[n] = lax.dynamic_index_in_dim(_split_shards(flat.reshape(shape), axis), dev, 0, keepdims=False
                                                ).reshape(inp[n].shape)
    tiny_names = [n for n in WEIGHT_NAMES if n not in BIG_SHARDED]
    upd = {}
    for n in big_names:
        upd[n] = _adamw(inp[n], grads[n], inp['m_' + n], inp['v_' + n], "adamw_" + n)
    tiny_sizes = [int(np.prod(inp[n].shape)) for n in tiny_names]
    tiny_offs = _row_offsets(tiny_sizes)
    packs = [_pack_rows([src(n).astype(F32).reshape(-1) for n in tiny_names], 8)
             for src in (lambda n: inp[n], lambda n: grads[n], lambda n: inp['m_' + n], lambda n: inp['v_' + n])]
    tiny_out = _adamw(*packs, "adamw_small")
    for n, off, size in zip(tiny_names, tiny_offs, tiny_sizes):
        r = _round_up(size, PACK_COLS) // PACK_COLS
        upd[n] = tuple(o[off:off + r].reshape(-1)[:size].reshape(inp[n].shape) for o in tiny_out)

    loss = lax.psum(loss_part[0, 0], ("x", "y", "c"))
    outs = [loss, grad_x[None]]
    outs += [grads[n] for n in WEIGHT_NAMES]
    for i in range(3):
        outs += [upd[n][i] for n in WEIGHT_NAMES]
    return tuple(outs)


def _step(inp):
    full = _gather_weights(inp)
    loss_part, grad_x, big_grads, rep_grads, small_grads = _local_step(inp, full)
    big_reduced, misc_sum, misc_sizes = _reduce_gradients(big_grads, rep_grads, small_grads)
    return _update(inp, loss_part, grad_x, big_reduced, misc_sum, misc_sizes)


INPUT_NAMES = (['x', 'mem'] + WEIGHT_NAMES + ['loss_target'] + ['m_' + n for n in WEIGHT_NAMES]
               + ['v_' + n for n in WEIGHT_NAMES])


def kernel(x, mem, norm_mix_g, norm_xa_g, norm_ffn_g, norm_mem_g, norm_final_g, w_in_ab, conv_qkv_a, a_log_a, dt_bias_a, onorm_g_a, ssm_lambda_re, ssm_lambda_im, ssm_b_re, ssm_b_im, ssm_c_re, ssm_c_im, ssm_d, ssm_log_dt, w_glu_b, b_glu_b, w_out_ab, pool_w, pool_scale, xa_wq, xa_wkv, xa_wo, ffn_w_up, ffn_conv, ffn_w_down, loss_target, m_norm_mix_g, m_norm_xa_g, m_norm_ffn_g, m_norm_mem_g, m_norm_final_g, m_w_in_ab, m_conv_qkv_a, m_a_log_a, m_dt_bias_a, m_onorm_g_a, m_ssm_lambda_re, m_ssm_lambda_im, m_ssm_b_re, m_ssm_b_im, m_ssm_c_re, m_ssm_c_im, m_ssm_d, m_ssm_log_dt, m_w_glu_b, m_b_glu_b, m_w_out_ab, m_pool_w, m_pool_scale, m_xa_wq, m_xa_wkv, m_xa_wo, m_ffn_w_up, m_ffn_conv, m_ffn_w_down, v_norm_mix_g, v_norm_xa_g, v_norm_ffn_g, v_norm_mem_g, v_norm_final_g, v_w_in_ab, v_conv_qkv_a, v_a_log_a, v_dt_bias_a, v_onorm_g_a, v_ssm_lambda_re, v_ssm_lambda_im, v_ssm_b_re, v_ssm_b_im, v_ssm_c_re, v_ssm_c_im, v_ssm_d, v_ssm_log_dt, v_w_glu_b, v_b_glu_b, v_w_out_ab, v_pool_w, v_pool_scale, v_xa_wq, v_xa_wkv, v_xa_wo, v_ffn_w_up, v_ffn_conv, v_ffn_w_down):
    args = (x, mem, norm_mix_g, norm_xa_g, norm_ffn_g, norm_mem_g, norm_final_g, w_in_ab, conv_qkv_a, a_log_a, dt_bias_a, onorm_g_a, ssm_lambda_re, ssm_lambda_im, ssm_b_re, ssm_b_im, ssm_c_re, ssm_c_im, ssm_d, ssm_log_dt, w_glu_b, b_glu_b, w_out_ab, pool_w, pool_scale, xa_wq, xa_wkv, xa_wo, ffn_w_up, ffn_conv, ffn_w_down, loss_target, m_norm_mix_g, m_norm_xa_g, m_norm_ffn_g, m_norm_mem_g, m_norm_final_g, m_w_in_ab, m_conv_qkv_a, m_a_log_a, m_dt_bias_a, m_onorm_g_a, m_ssm_lambda_re, m_ssm_lambda_im, m_ssm_b_re, m_ssm_b_im, m_ssm_c_re, m_ssm_c_im, m_ssm_d, m_ssm_log_dt, m_w_glu_b, m_b_glu_b, m_w_out_ab, m_pool_w, m_pool_scale, m_xa_wq, m_xa_wkv, m_xa_wo, m_ffn_w_up, m_ffn_conv, m_ffn_w_down, v_norm_mix_g, v_norm_xa_g, v_norm_ffn_g, v_norm_mem_g, v_norm_final_g, v_w_in_ab, v_conv_qkv_a, v_a_log_a, v_dt_bias_a, v_onorm_g_a, v_ssm_lambda_re, v_ssm_lambda_im, v_ssm_b_re, v_ssm_b_im, v_ssm_c_re, v_ssm_c_im, v_ssm_d, v_ssm_log_dt, v_w_glu_b, v_b_glu_b, v_w_out_ab, v_pool_w, v_pool_scale, v_xa_wq, v_xa_wkv, v_xa_wo, v_ffn_w_up, v_ffn_conv, v_ffn_w_down)
    return _step(dict(zip(INPUT_NAMES, args)))
```

```python
import functools
import math

import numpy as np
import jax
import jax.numpy as jnp
from jax import lax
from jax.experimental import pallas as pl
from jax.experimental.pallas import tpu as pltpu

F32, BF16 = jnp.float32, jnp.bfloat16
HIGH, HIGHEST = lax.Precision.HIGH, lax.Precision.HIGHEST
MESH = pl.DeviceIdType.MESH

N_DEV = 8
SEQ, D_MODEL, MEM_LEN = 2048, 1024, 256
WIDTH_A, N_HEADS_A, HEAD_A, CONV_A = 512, 4, 128, 4
GDR_CHUNK = 128
SSM_WIDTH, SSM_GROUP, N_GROUPS, SSM_STATE = 512, 16, 32, 64
SSM_CH = N_GROUPS * SSM_STATE
SCAN_CB = 512
POOL_WINDOWS = (2, 4, 8, 16)
POOL_GROUP = 256
N_HEADS_X, HEAD_X = 4, 256
D_FF, CONV_FFN = 2816, 3
RMS_EPS = 1e-6
ADAM_LR, ADAM_B1, ADAM_B2, ADAM_EPS, ADAM_WD, ADAM_STEP = 0.001, 0.9, 0.999, 1e-08, 0.01, 10
LANE = 128
PACK_COLS = 1024
VMEM_LIMIT_BYTES = 56 * 1024 * 1024


def _params(sem=None):
    return pltpu.CompilerParams(dimension_semantics=sem, vmem_limit_bytes=VMEM_LIMIT_BYTES)


def _tile(dim, pref):
    best = None
    for t in range(LANE, min(dim, pref) + 1, LANE):
        if dim % t == 0:
            best = t
    return best if best is not None else dim


def _mm(a, b, mode, name, out_dtype=F32, res=None, tm=512, tn=512, tk=1024):
    if mode == "nn":
        (m, k), n = a.shape, b.shape[1]
    elif mode == "nt":
        (m, k), n = a.shape, b.shape[0]
    else:
        (k, m), n = a.shape, b.shape[1]
    tm, tn, tk = _tile(m, tm), _tile(n, tn), _tile(k, tk)
    nk = k // tk
    dims = {"nn": ((1,), (0,)), "nt": ((1,), (1,)), "tn": ((0,), (0,))}[mode]

    def body(*refs):
        if res is None:
            a_ref, b_ref, o_ref, acc = refs
            r_ref = None
        else:
            a_ref, b_ref, r_ref, o_ref, acc = refs
        kk = pl.program_id(2)

        @pl.when(kk == 0)
        def _():
            acc[...] = jnp.zeros_like(acc)

        acc[...] += lax.dot_general(a_ref[...].astype(BF16), b_ref[...].astype(BF16), (dims, ((), ())),
                                    preferred_element_type=F32)

        @pl.when(kk == nk - 1)
        def _():
            out = acc[...]
            if r_ref is not None:
                out = out + r_ref[...].astype(F32)
            o_ref[...] = out.astype(out_dtype)

    a_spec = (pl.BlockSpec((tk, tm), lambda i, j, q: (q, i)) if mode == "tn"
              else pl.BlockSpec((tm, tk), lambda i, j, q: (i, q)))
    b_spec = (pl.BlockSpec((tn, tk), lambda i, j, q: (j, q)) if mode == "nt"
              else pl.BlockSpec((tk, tn), lambda i, j, q: (q, j)))
    o_spec = pl.BlockSpec((tm, tn), lambda i, j, q: (i, j))
    in_specs, args = [a_spec, b_spec], [a, b]
    if res is not None:
        in_specs.append(o_spec)
        args.append(res)
    return pl.pallas_call(
        body, name=name, grid=(m // tm, n // tn, nk), in_specs=in_specs, out_specs=o_spec,
        out_shape=jax.ShapeDtypeStruct((m, n), out_dtype), scratch_shapes=[pltpu.VMEM((tm, tn), F32)],
        compiler_params=_params(("parallel", "parallel", "arbitrary")))(*args)


def _rms_fwd(x, g, out_dtype, name, tr=256):
    rows, d = x.shape

    def body(x_ref, g_ref, o_ref):
        xv = x_ref[...]
        r = lax.rsqrt(jnp.mean(xv * xv, axis=-1, keepdims=True) + RMS_EPS)
        o_ref[...] = (xv * r * g_ref[...]).astype(out_dtype)

    return pl.pallas_call(
        body, name=name, grid=(rows // tr,),
        in_specs=[pl.BlockSpec((tr, d), lambda i: (i, 0)), pl.BlockSpec((1, d), lambda i: (0, 0))],
        out_specs=pl.BlockSpec((tr, d), lambda i: (i, 0)), out_shape=jax.ShapeDtypeStruct((rows, d), out_dtype),
        compiler_params=_params(("parallel",)))(x, g.reshape(1, d))


def _rms_bwd(x, g, dy, dres, name, tr=256):
    rows, d = x.shape

    def body(*refs):
        if dres is None:
            x_ref, g_ref, dy_ref, dx_ref, dg_ref = refs
            r_ref = None
        else:
            x_ref, g_ref, dy_ref, r_ref, dx_ref, dg_ref = refs

        @pl.when(pl.program_id(0) == 0)
        def _():
            dg_ref[...] = jnp.zeros_like(dg_ref)

        xv, dyv = x_ref[...], dy_ref[...].astype(F32)
        r = lax.rsqrt(jnp.mean(xv * xv, axis=-1, keepdims=True) + RMS_EPS)
        xh = xv * r
        dyg = dyv * g_ref[...]
        dx = r * (dyg - xh * jnp.mean(dyg * xh, axis=-1, keepdims=True))
        if r_ref is not None:
            dx = dx + r_ref[...]
        dx_ref[...] = dx
        dg_ref[...] += jnp.sum(dyv * xh, axis=0, keepdims=True)

    blk = pl.BlockSpec((tr, d), lambda i: (i, 0))
    vec = pl.BlockSpec((1, d), lambda i: (0, 0))
    in_specs, args = [blk, vec, blk], [x, g.reshape(1, d), dy]
    if dres is not None:
        in_specs.append(blk)
        args.append(dres)
    return pl.pallas_call(
        body, name=name, grid=(rows // tr,), in_specs=in_specs, out_specs=[blk, vec],
        out_shape=[jax.ShapeDtypeStruct((rows, d), F32), jax.ShapeDtypeStruct((1, d), F32)],
        compiler_params=_params(("arbitrary",)))(*args)


def _loss_head(x, g, target, name, tr=256):
    rows, d = x.shape

    def body(x_ref, g_ref, t_ref, loss_ref, dx_ref, dg_ref):
        @pl.when(pl.program_id(0) == 0)
        def _():
            dg_ref[...] = jnp.zeros_like(dg_ref)
            loss_ref[...] = jnp.zeros_like(loss_ref)

        xv = x_ref[...]
        r = lax.rsqrt(jnp.mean(xv * xv, axis=-1, keepdims=True) + RMS_EPS)
        xh = xv * r
        err = xh * g_ref[...] - t_ref[...]
        loss_ref[...] += 0.5 * jnp.sum(jnp.mean(err * err, axis=-1, keepdims=True), keepdims=True)
        dyv = err * (1.0 / d)
        dyg = dyv * g_ref[...]
        dx_ref[...] = r * (dyg - xh * jnp.mean(dyg * xh, axis=-1, keepdims=True))
        dg_ref[...] += jnp.sum(dyv * xh, axis=0, keepdims=True)

    blk = pl.BlockSpec((tr, d), lambda i: (i, 0))
    vec = pl.BlockSpec((1, d), lambda i: (0, 0))
    return pl.pallas_call(
        body, name=name, grid=(rows // tr,), in_specs=[blk, vec, blk],
        out_specs=[pl.BlockSpec((1, 1), lambda i: (0, 0)), blk, vec],
        out_shape=[jax.ShapeDtypeStruct((1, 1), F32), jax.ShapeDtypeStruct((rows, d), F32),
                   jax.ShapeDtypeStruct((1, d), F32)],
        compiler_params=_params(("arbitrary",)))(x, g.reshape(1, d), target)


def _shift_down(x, s):
    rows = lax.broadcasted_iota(jnp.int32, x.shape, 0)
    return jnp.where(rows >= s, pltpu.roll(x, s, 0), 0.0)


def _shift_up(x, s):
    n = x.shape[0]
    rows = lax.broadcasted_iota(jnp.int32, x.shape, 0)
    return jnp.where(rows < n - s, pltpu.roll(x, n - s, 0), 0.0)


def _sigmoid(x):
    return 1.0 / (1.0 + jnp.exp(-x))


def _silu_and_grad(x):
    s = _sigmoid(x)
    return x * s, s * (1.0 + x * (1.0 - s))


_GELU_C0, _GELU_C1 = math.sqrt(2.0 / math.pi), 0.044715


def _gelu_and_grad(x):
    th = jnp.tanh(_GELU_C0 * (x + _GELU_C1 * x * x * x))
    y = 0.5 * x * (1.0 + th)
    dy = 0.5 * (1.0 + th) + 0.5 * x * (1.0 - th * th) * _GELU_C0 * (1.0 + 3.0 * _GELU_C1 * x * x)
    return y, dy


def _ffn_act_fwd(h, w, name, tc=256):
    t = h.shape[0]
    nb = D_FF // tc

    def body(hg_ref, hv_ref, wg_ref, wv_ref, a_ref):
        def conv(x, wr):
            return wr[2:3, :] * x + wr[1:2, :] * _shift_down(x, 1) + wr[0:1, :] * _shift_down(x, 2)

        cg = conv(hg_ref[...], wg_ref[...])
        cv = conv(hv_ref[...], wv_ref[...])
        a_ref[...] = (cg * _sigmoid(cg) * cv).astype(BF16)

    return pl.pallas_call(
        body, name=name, grid=(nb,),
        in_specs=[pl.BlockSpec((t, tc), lambda j: (0, j)), pl.BlockSpec((t, tc), lambda j: (0, j + nb)),
                  pl.BlockSpec((CONV_FFN, tc), lambda j: (0, j)), pl.BlockSpec((CONV_FFN, tc), lambda j: (0, j + nb))],
        out_specs=pl.BlockSpec((t, tc), lambda j: (0, j)), out_shape=jax.ShapeDtypeStruct((t, D_FF), BF16),
        compiler_params=_params(("parallel",)))(h, h, w, w)


def _ffn_act_bwd(h, w, da, name, tc=256):
    t = h.shape[0]
    nb = D_FF // tc

    def body(hg_ref, hv_ref, wg_ref, wv_ref, da_ref, dhg_ref, dhv_ref, dwg_ref, dwv_ref):
        hg, hv, wg, wv = hg_ref[...], hv_ref[...], wg_ref[...], wv_ref[...]
        hg1, hg2, hv1, hv2 = _shift_down(hg, 1), _shift_down(hg, 2), _shift_down(hv, 1), _shift_down(hv, 2)
        cg = wg[2:3, :] * hg + wg[1:2, :] * hg1 + wg[0:1, :] * hg2
        cv = wv[2:3, :] * hv + wv[1:2, :] * hv1 + wv[0:1, :] * hv2
        sg, dsg = _silu_and_grad(cg)
        dav = da_ref[...].astype(F32)
        dcv = dav * sg
        dcg = dav * cv * dsg

        def conv_t(dc, wr):
            return wr[2:3, :] * dc + wr[1:2, :] * _shift_up(dc, 1) + wr[0:1, :] * _shift_up(dc, 2)

        dhg_ref[...] = conv_t(dcg, wg).astype(BF16)
        dhv_ref[...] = conv_t(dcv, wv).astype(BF16)
        dwg_ref[0:1, :] = jnp.sum(dcg * hg2, axis=0, keepdims=True)
        dwg_ref[1:2, :] = jnp.sum(dcg * hg1, axis=0, keepdims=True)
        dwg_ref[2:3, :] = jnp.sum(dcg * hg, axis=0, keepdims=True)
        dwv_ref[0:1, :] = jnp.sum(dcv * hv2, axis=0, keepdims=True)
        dwv_ref[1:2, :] = jnp.sum(dcv * hv1, axis=0, keepdims=True)
        dwv_ref[2:3, :] = jnp.sum(dcv * hv, axis=0, keepdims=True)

    big = lambda off: pl.BlockSpec((t, tc), lambda j: (0, j + off))
    small = lambda off: pl.BlockSpec((CONV_FFN, tc), lambda j: (0, j + off))
    dhg, dhv, dwg, dwv = pl.pallas_call(
        body, name=name, grid=(nb,),
        in_specs=[big(0), big(nb), small(0), small(nb), big(0)],
        out_specs=[big(0), big(0), small(0), small(0)],
        out_shape=[jax.ShapeDtypeStruct((t, D_FF), BF16), jax.ShapeDtypeStruct((t, D_FF), BF16),
                   jax.ShapeDtypeStruct((CONV_FFN, D_FF), F32), jax.ShapeDtypeStruct((CONV_FFN, D_FF), F32)],
        compiler_params=_params(("parallel",)))(h, h, w, w, da)
    return jnp.concatenate([dhg, dhv], axis=1), jnp.concatenate([dwg, dwv], axis=1)


def _attn_probs(q, k):
    s = lax.dot_general(q.astype(BF16), k.astype(BF16), (((1,), (1,)), ((), ())),
                        preferred_element_type=F32) * (HEAD_X ** -0.5)
    s = s - jnp.max(s, axis=-1, keepdims=True)
    p = jnp.exp(s)
    return p / jnp.sum(p, axis=-1, keepdims=True)


def _attn_fwd(q, kv, name, tq=512):
    t = q.shape[0]

    def body(q_ref, k_ref, v_ref, o_ref):
        p = _attn_probs(q_ref[...], k_ref[...])
        o_ref[...] = jnp.dot(p.astype(BF16), v_ref[...].astype(BF16), preferred_element_type=F32).astype(BF16)

    return pl.pallas_call(
        body, name=name, grid=(N_HEADS_X, t // tq),
        in_specs=[pl.BlockSpec((tq, HEAD_X), lambda h, i: (i, h)),
                  pl.BlockSpec((MEM_LEN, HEAD_X), lambda h, i: (0, h)),
                  pl.BlockSpec((MEM_LEN, HEAD_X), lambda h, i: (0, h + N_HEADS_X))],
        out_specs=pl.BlockSpec((tq, HEAD_X), lambda h, i: (i, h)),
        out_shape=jax.ShapeDtypeStruct((t, N_HEADS_X * HEAD_X), BF16),
        compiler_params=_params(("parallel", "parallel")))(q, kv, kv)


def _attn_bwd(q, kv, do, name, tq=512):
    t = q.shape[0]

    def body(q_ref, k_ref, v_ref, do_ref, dq_ref, dk_ref, dv_ref):
        @pl.when(pl.program_id(1) == 0)
        def _():
            dk_ref[...] = jnp.zeros_like(dk_ref)
            dv_ref[...] = jnp.zeros_like(dv_ref)

        qb, kb, vb, dob = (r[...].astype(BF16) for r in (q_ref, k_ref, v_ref, do_ref))
        p = _attn_probs(qb, kb)
        dp = lax.dot_general(dob, vb, (((1,), (1,)), ((), ())), preferred_element_type=F32)
        ds = p * (dp - jnp.sum(dp * p, axis=-1, keepdims=True)) * (HEAD_X ** -0.5)
        dsb = ds.astype(BF16)
        dq_ref[...] = jnp.dot(dsb, kb, preferred_element_type=F32).astype(BF16)
        dk_ref[...] += lax.dot_general(dsb, qb, (((0,), (0,)), ((), ())), preferred_element_type=F32)
        dv_ref[...] += lax.dot_general(p.astype(BF16), dob, (((0,), (0,)), ((), ())), preferred_element_type=F32)

    qs = pl.BlockSpec((tq, HEAD_X), lambda h, i: (i, h))
    ms = pl.BlockSpec((MEM_LEN, HEAD_X), lambda h, i: (0, h))
    return pl.pallas_call(
        body, name=name, grid=(N_HEADS_X, t // tq),
        in_specs=[qs, ms, pl.BlockSpec((MEM_LEN, HEAD_X), lambda h, i: (0, h + N_HEADS_X)), qs],
        out_specs=[qs, ms, ms],
        out_shape=[jax.ShapeDtypeStruct((t, D_MODEL), BF16), jax.ShapeDtypeStruct((MEM_LEN, D_MODEL), F32),
                   jax.ShapeDtypeStruct((MEM_LEN, D_MODEL), F32)],
        compiler_params=_params(("parallel", "arbitrary")))(q, kv, kv, do)


def _pool_counts(t, win):
    pos = lax.broadcasted_iota(jnp.int32, (t, 1), 0).astype(F32) + 1.0
    return 1.0 / jnp.minimum(pos, float(win))


def _pool_delta(xv, win):
    s, step = xv, 1
    while step < win:
        s = s + _shift_down(s, step)
        step *= 2
    return s * _pool_counts(xv.shape[0], win) - xv


def _pool_delta_t(dv, win):
    s, step = dv * _pool_counts(dv.shape[0], win), 1
    while step < win:
        s = s + _shift_up(s, step)
        step *= 2
    return s - dv


def _pool_fwd(xn, w, scale, res, name):
    t = xn.shape[0]

    def make_branch(win, xn_ref, w_ref, s_ref, r_ref, o_ref):
        def branch():
            dl = _pool_delta(xn_ref[...], win)
            y = jnp.dot(dl.astype(BF16), w_ref[0], preferred_element_type=F32)
            o_ref[...] = r_ref[...] + y * s_ref[...]
        return branch

    def body(xn_ref, w_ref, s_ref, r_ref, o_ref):
        for gi, win in enumerate(POOL_WINDOWS):
            pl.when(pl.program_id(0) == gi)(make_branch(win, xn_ref, w_ref, s_ref, r_ref, o_ref))

    blk = pl.BlockSpec((t, POOL_GROUP), lambda g: (0, g))
    return pl.pallas_call(
        body, name=name, grid=(len(POOL_WINDOWS),),
        in_specs=[blk, pl.BlockSpec((1, POOL_GROUP, POOL_GROUP), lambda g: (g, 0, 0)),
                  pl.BlockSpec((1, POOL_GROUP), lambda g: (0, g)), blk],
        out_specs=blk, out_shape=jax.ShapeDtypeStruct((t, D_MODEL), F32),
        compiler_params=_params(("parallel",)))(xn, w, scale, res)


def _pool_bwd(xn, w, scale, dmix, name):
    t = xn.shape[0]

    def make_branch(win, xn_ref, w_ref, s_ref, d_ref, dxn_ref, dw_ref, ds_ref):
        def branch():
            dl = _pool_delta(xn_ref[...], win).astype(BF16)
            wv = w_ref[0]
            dm = d_ref[...]
            y = jnp.dot(dl, wv, preferred_element_type=F32)
            ds_ref[...] = jnp.sum(dm * y, axis=0, keepdims=True)
            dy = (dm * s_ref[...]).astype(BF16)
            dw_ref[0] = lax.dot_general(dl, dy, (((0,), (0,)), ((), ())), preferred_element_type=F32)
            ddl = lax.dot_general(dy, wv, (((1,), (1,)), ((), ())), preferred_element_type=F32)
            dxn_ref[...] = _pool_delta_t(ddl, win)
        return branch

    def body(*refs):
        for gi, win in enumerate(POOL_WINDOWS):
            pl.when(pl.program_id(0) == gi)(make_branch(win, *refs))

    blk = pl.BlockSpec((t, POOL_GROUP), lambda g: (0, g))
    wspec = pl.BlockSpec((1, POOL_GROUP, POOL_GROUP), lambda g: (g, 0, 0))
    vec = pl.BlockSpec((1, POOL_GROUP), lambda g: (0, g))
    return pl.pallas_call(
        body, name=name, grid=(len(POOL_WINDOWS),), in_specs=[blk, wspec, vec, blk], out_specs=[blk, wspec, vec],
        out_shape=[jax.ShapeDtypeStruct((t, D_MODEL), F32),
                   jax.ShapeDtypeStruct((len(POOL_WINDOWS), POOL_GROUP, POOL_GROUP), F32),
                   jax.ShapeDtypeStruct((1, D_MODEL), F32)],
        compiler_params=_params(("parallel",)))(xn, w, scale, dmix)


def _qkv_conv(h, wr):
    return (wr[3:4, :] * h + wr[2:3, :] * _shift_down(h, 1) + wr[1:2, :] * _shift_down(h, 2)
            + wr[0:1, :] * _shift_down(h, 3))


def _qkv_pre_fwd(h, w, col0, ncols, normalize, scale, name):
    t = h.shape[0]

    def body(h_ref, w_ref, o_ref):
        c = _qkv_conv(h_ref[...], w_ref[...])
        s = c * _sigmoid(c)
        if normalize:
            s = s * lax.rsqrt(jnp.sum(s * s, axis=-1, keepdims=True) + 1e-6) * scale
        o_ref[...] = s

    return pl.pallas_call(
        body, name=name, grid=(ncols,),
        in_specs=[pl.BlockSpec((t, HEAD_A), lambda j: (0, j + col0)), pl.BlockSpec((CONV_A, HEAD_A), lambda j: (0, j + col0))],
        out_specs=pl.BlockSpec((t, HEAD_A), lambda j: (0, j)), out_shape=jax.ShapeDtypeStruct((t, ncols * HEAD_A), F32),
        compiler_params=_params(("parallel",)))(h, w)


def _qkv_pre_bwd(h, w, dy, col0, ncols, normalize, scale, name):
    t = h.shape[0]

    def body(h_ref, w_ref, dy_ref, dh_ref, dw_ref):
        hv, wr, dyv = h_ref[...], w_ref[...], dy_ref[...]
        h1, h2, h3 = _shift_down(hv, 1), _shift_down(hv, 2), _shift_down(hv, 3)
        c = wr[3:4, :] * hv + wr[2:3, :] * h1 + wr[1:2, :] * h2 + wr[0:1, :] * h3
        s, dsilu = _silu_and_grad(c)
        if normalize:
            r = lax.rsqrt(jnp.sum(s * s, axis=-1, keepdims=True) + 1e-6)
            y = s * r
            dyv = dyv * scale
            ds = r * (dyv - y * jnp.sum(dyv * y, axis=-1, keepdims=True))
        else:
            ds = dyv
        dc = ds * dsilu
        dh = (wr[3:4, :] * dc + wr[2:3, :] * _shift_up(dc, 1) + wr[1:2, :] * _shift_up(dc, 2)
              + wr[0:1, :] * _shift_up(dc, 3))
        dh_ref[...] = dh.astype(BF16)
        dw_ref[0:1, :] = jnp.sum(dc * h3, axis=0, keepdims=True)
        dw_ref[1:2, :] = jnp.sum(dc * h2, axis=0, keepdims=True)
        dw_ref[2:3, :] = jnp.sum(dc * h1, axis=0, keepdims=True)
        dw_ref[3:4, :] = jnp.sum(dc * hv, axis=0, keepdims=True)

    return pl.pallas_call(
        body, name=name, grid=(ncols,),
        in_specs=[pl.BlockSpec((t, HEAD_A), lambda j: (0, j + col0)), pl.BlockSpec((CONV_A, HEAD_A), lambda j: (0, j + col0)),
                  pl.BlockSpec((t, HEAD_A), lambda j: (0, j))],
        out_specs=[pl.BlockSpec((t, HEAD_A), lambda j: (0, j)), pl.BlockSpec((CONV_A, HEAD_A), lambda j: (0, j))],
        out_shape=[jax.ShapeDtypeStruct((t, ncols * HEAD_A), BF16), jax.ShapeDtypeStruct((CONV_A, ncols * HEAD_A), F32)],
        compiler_params=_params(("parallel",)))(h, w, dy)


def _softplus(x):
    return jnp.maximum(x, 0.0) + jnp.log1p(jnp.exp(-jnp.abs(x)))


def _gates_fwd(ba, arow, brow, name):
    t = ba.shape[0]

    def body(x_ref, a_ref, b_ref, o_ref):
        xv = x_ref[...]
        lane = lax.broadcasted_iota(jnp.int32, xv.shape, 1)
        beta = _sigmoid(xv)
        g = -jnp.exp(a_ref[...]) * _softplus(xv + b_ref[...])
        o_ref[...] = jnp.where(lane < N_HEADS_A, beta, jnp.where(lane < 2 * N_HEADS_A, g, 0.0))

    return pl.pallas_call(body, name=name, out_shape=jax.ShapeDtypeStruct((t, LANE), F32),
                          compiler_params=_params())(ba, arow, brow)


def _gates_bwd(ba, arow, brow, dgb, name):
    t = ba.shape[0]

    def body(x_ref, a_ref, b_ref, d_ref, dx_ref, da_ref, db_ref):
        xv, dv = x_ref[...], d_ref[...]
        lane = lax.broadcasted_iota(jnp.int32, xv.shape, 1)
        beta = _sigmoid(xv)
        ea = jnp.exp(a_ref[...])
        z = xv + b_ref[...]
        dgv = jnp.where((lane >= N_HEADS_A) & (lane < 2 * N_HEADS_A), dv, 0.0) * (-ea)
        dz = dgv * _sigmoid(z)
        dx = jnp.where(lane < N_HEADS_A, dv * beta * (1.0 - beta), dz)
        dx_ref[...] = dx.astype(BF16)
        db_ref[...] = jnp.sum(dz, axis=0, keepdims=True)
        da_ref[...] = jnp.sum(dgv * _softplus(z), axis=0, keepdims=True)

    return pl.pallas_call(
        body, name=name,
        out_shape=[jax.ShapeDtypeStruct((t, LANE), BF16), jax.ShapeDtypeStruct((1, LANE), F32),
                   jax.ShapeDtypeStruct((1, LANE), F32)],
        compiler_params=_params())(ba, arow, brow, dgb)


def _dot(a, b, prec=None):
    if prec is None:
        return jnp.dot(a.astype(BF16), b.astype(BF16), preferred_element_type=F32)
    return jnp.dot(a, b, precision=prec, preferred_element_type=F32)


def _dot_nt(a, b, prec=None):
    if prec is None:
        a, b = a.astype(BF16), b.astype(BF16)
    return lax.dot_general(a, b, (((1,), (1,)), ((), ())), precision=prec, preferred_element_type=F32)


def _dot_tn(a, b, prec=None):
    if prec is None:
        a, b = a.astype(BF16), b.astype(BF16)
    return lax.dot_general(a, b, (((0,), (0,)), ((), ())), precision=prec, preferred_element_type=F32)


def _gdr_chunk_terms(k, beta, g):
    c = GDR_CHUNK
    row = lax.broadcasted_iota(jnp.int32, (c, c), 0)
    col = lax.broadcasted_iota(jnp.int32, (c, c), 1)
    causal, strict = row >= col, row > col
    gcum = _dot(causal.astype(F32), jnp.broadcast_to(g, (c, c)), HIGHEST)
    diff = gcum - gcum.T
    decay = jnp.where(causal, jnp.exp(jnp.where(causal, diff, 0.0)), 0.0)
    kb = k * beta
    kk = _dot_nt(kb, k)
    return row, col, causal, strict, gcum, decay, kb, kk


def _unit_lower_inverse(a):
    c = a.shape[0]
    eye = (lax.broadcasted_iota(jnp.int32, (c, c), 0) == lax.broadcasted_iota(jnp.int32, (c, c), 1)).astype(F32)
    p = -a
    inv = eye + p
    step = 1
    while 2 * step < c:
        p = _dot(p, p, HIGH)
        inv = inv + _dot(inv, p, HIGH)
        step *= 2
    return inv


def _gdr_fwd(q, k, v, gb, name):
    t = q.shape[0]
    c = GDR_CHUNK
    n = t // c

    def body(q_ref, k_ref, v_ref, gb_ref, o_ref, tm_ref, s_ref, state):
        @pl.when(pl.program_id(1) == 0)
        def _():
            state[...] = jnp.zeros_like(state)

        qv, kv, vv = q_ref[...], k_ref[...], v_ref[...]
        beta, g = gb_ref[0, :, 0:1], gb_ref[0, :, 1:2]
        row, col, causal, strict, gcum, decay, kb, kk = _gdr_chunk_terms(kv, beta, g)
        tm = _unit_lower_inverse(jnp.where(strict, kk * decay, 0.0))
        e = jnp.exp(gcum)
        u = _dot(tm, vv * beta, HIGH)
        w = _dot(tm, kb * e, HIGH)
        p = jnp.where(causal, _dot_nt(qv, kv) * decay, 0.0)
        s = state[...]
        s_ref[0, 0] = s
        tm_ref[0, 0] = tm
        vn = u - _dot(w, s)
        o_ref[...] = _dot(qv * e, s) + _dot(p, vn)
        glast = gcum[c - 1:c, :]
        state[...] = s * jnp.exp(glast) + _dot_tn(kv * jnp.exp(glast - gcum), vn)

    blk = pl.BlockSpec((c, HEAD_A), lambda h, i: (i, h))
    mat = pl.BlockSpec((1, 1, c, c), lambda h, i: (h, i, 0, 0))
    return pl.pallas_call(
        body, name=name, grid=(N_HEADS_A, n),
        in_specs=[blk, blk, blk, pl.BlockSpec((1, c, 2), lambda h, i: (h, i, 0))],
        out_specs=[blk, mat, mat],
        out_shape=[jax.ShapeDtypeStruct((t, WIDTH_A), F32), jax.ShapeDtypeStruct((N_HEADS_A, n, c, c), F32),
                   jax.ShapeDtypeStruct((N_HEADS_A, n, HEAD_A, HEAD_A), F32)],
        scratch_shapes=[pltpu.VMEM((HEAD_A, HEAD_A), F32)],
        compiler_params=_params(("parallel", "arbitrary")))(q, k, v, gb)


def _gdr_bwd(q, k, v, gb, tm_all, s_all, do, name):
    t = q.shape[0]
    c = GDR_CHUNK
    n = t // c

    def body(q_ref, k_ref, v_ref, gb_ref, tm_ref, s_ref, do_ref, dq_ref, dk_ref, dv_ref, dgb_ref, dstate):
        @pl.when(pl.program_id(1) == 0)
        def _():
            dstate[...] = jnp.zeros_like(dstate)

        qv, kv, vv, dov = q_ref[...], k_ref[...], v_ref[...], do_ref[...]
        beta, g = gb_ref[0, :, 0:1], gb_ref[0, :, 1:2]
        tm, s, dsp = tm_ref[0, 0], s_ref[0, 0], dstate[...]
        row, col, causal, strict, gcum, decay, kb, kk = _gdr_chunk_terms(kv, beta, g)
        e = jnp.exp(gcum)
        vb, kbe = vv * beta, kb * e
        u = _dot(tm, vb, HIGH)
        w = _dot(tm, kbe, HIGH)
        qk = _dot_nt(qv, kv)
        p = jnp.where(causal, qk * decay, 0.0)
        vn = u - _dot(w, s)
        glast = gcum[c - 1:c, :]
        el = jnp.exp(glast)
        f = jnp.exp(glast - gcum)
        kd = kv * f
        qe = qv * e

        dvn = _dot_tn(p, dov) + _dot(kd, dsp)
        dglast = el[:, 0:1] * jnp.sum(s * dsp, keepdims=True)
        dkd = _dot_nt(vn, dsp)
        dk = dkd * f
        df = jnp.sum(dkd * kv, axis=1, keepdims=True) * f[:, 0:1]
        dglast = dglast + jnp.sum(df, keepdims=True)
        dgc = -df
        dp = jnp.where(causal, _dot_nt(dov, vn), 0.0)
        dqe = _dot_nt(dov, s)
        dq = dqe * e
        de = jnp.sum(dqe * qv, axis=1, keepdims=True)
        dstate[...] = dsp * el + _dot_tn(qe, dov) - _dot_tn(w, dvn)
        dw = -_dot_nt(dvn, s)
        dvb = _dot_tn(tm, dvn, HIGH)
        dkbe = _dot_tn(tm, dw, HIGH)
        da = -jnp.where(strict, _dot_nt(dvb, u) + _dot_nt(dkbe, w), 0.0)
        dkk = da * decay
        dqk = dp * decay
        dd = da * kk + dp * qk
        dq = dq + _dot(dqk, kv)
        dk = dk + _dot_tn(dqk, qv)
        dkb = _dot(dkk, kv) + dkbe * e
        dk = dk + _dot_tn(dkk, kb)
        de = de + jnp.sum(dkbe * kb, axis=1, keepdims=True)
        dk = dk + dkb * beta
        dbeta = jnp.sum(dkb * kv, axis=1, keepdims=True) + jnp.sum(dvb * vv, axis=1, keepdims=True)
        m = dd * decay
        dgc = dgc + jnp.sum(m, axis=1, keepdims=True) - jnp.sum(m.T, axis=1, keepdims=True)
        dgc = dgc + de * e[:, 0:1]
        dgc = dgc + jnp.where(row[:, 0:1] == c - 1, dglast, 0.0)
        dg = _dot((row <= col).astype(F32), jnp.broadcast_to(dgc, (c, c)), HIGHEST)
        dq_ref[...] = dq
        dk_ref[...] = dk
        dv_ref[...] = dvb * beta
        dgb_ref[0, :, 0:1] = dbeta
        dgb_ref[0, :, 1:2] = dg[:, 0:1]

    blk = pl.BlockSpec((c, HEAD_A), lambda h, i: (n - 1 - i, h))
    mat = pl.BlockSpec((1, 1, c, c), lambda h, i: (h, n - 1 - i, 0, 0))
    gsp = pl.BlockSpec((1, c, 2), lambda h, i: (h, n - 1 - i, 0))
    return pl.pallas_call(
        body, name=name, grid=(N_HEADS_A, n),
        in_specs=[blk, blk, blk, gsp, mat, mat, blk], out_specs=[blk, blk, blk, gsp],
        out_shape=[jax.ShapeDtypeStruct((t, WIDTH_A), F32)] * 3 + [jax.ShapeDtypeStruct((N_HEADS_A, t, 2), F32)],
        scratch_shapes=[pltpu.VMEM((HEAD_A, HEAD_A), F32)],
        compiler_params=_params(("parallel", "arbitrary")))(q, k, v, gb, tm_all, s_all, do)


def _onorm_fwd(o, gate, g, name):
    t = o.shape[0]

    def body(o_ref, gate_ref, g_ref, y_ref):
        ov, gv = o_ref[...], gate_ref[...]
        r = lax.rsqrt(jnp.mean(ov * ov, axis=-1, keepdims=True) + RMS_EPS)
        y_ref[...] = (ov * r * g_ref[...] * gv * _sigmoid(gv)).astype(BF16)

    blk = pl.BlockSpec((t, HEAD_A), lambda j: (0, j))
    return pl.pallas_call(
        body, name=name, grid=(N_HEADS_A,), in_specs=[blk, blk, pl.BlockSpec((1, HEAD_A), lambda j: (0, 0))],
        out_specs=blk, out_shape=jax.ShapeDtypeStruct((t, WIDTH_A), BF16),
        compiler_params=_params(("parallel",)))(o, gate, g)


def _onorm_bwd(o, gate, g, dy, name):
    t = o.shape[0]

    def body(o_ref, gate_ref, g_ref, dy_ref, do_ref, dgate_ref, dg_ref):
        @pl.when(pl.program_id(0) == 0)
        def _():
            dg_ref[...] = jnp.zeros_like(dg_ref)

        ov, gv, dyv = o_ref[...], gate_ref[...], dy_ref[...].astype(F32)
        r = lax.rsqrt(jnp.mean(ov * ov, axis=-1, keepdims=True) + RMS_EPS)
        oh = ov * r
        sg, dsg = _silu_and_grad(gv)
        dgate_ref[...] = (dyv * oh * g_ref[...] * dsg).astype(BF16)
        dn = dyv * sg
        dg_ref[...] += jnp.sum(dn * oh, axis=0, keepdims=True)
        dng = dn * g_ref[...]
        do_ref[...] = r * (dng - oh * jnp.mean(dng * oh, axis=-1, keepdims=True))

    blk = pl.BlockSpec((t, HEAD_A), lambda j: (0, j))
    vec = pl.BlockSpec((1, HEAD_A), lambda j: (0, 0))
    return pl.pallas_call(
        body, name=name, grid=(N_HEADS_A,), in_specs=[blk, blk, vec, blk], out_specs=[blk, blk, vec],
        out_shape=[jax.ShapeDtypeStruct((t, WIDTH_A), F32), jax.ShapeDtypeStruct((t, WIDTH_A), BF16),
                   jax.ShapeDtypeStruct((1, HEAD_A), F32)],
        compiler_params=_params(("arbitrary",)))(o, gate, g, dy)


def _cmul(ar, ai, br, bi):
    return ar * br - ai * bi, ar * bi + ai * br


def _scan_tables(ar, ai, reverse):
    p1 = (ar, ai)
    p2 = _cmul(*p1, *p1)
    p4 = _cmul(*p2, *p2)
    p8 = _cmul(*p4, *p4)
    p3 = _cmul(*p2, *p1)
    p5 = _cmul(*p4, *p1)
    p6 = _cmul(*p4, *p2)
    p7 = _cmul(*p4, *p3)
    pows = [p1, p2, p3, p4, p5, p6, p7, p8]
    rows = lax.broadcasted_iota(jnp.int32, (8, ar.shape[1]), 0)
    tr = jnp.zeros((8, ar.shape[1]), F32)
    ti = jnp.zeros((8, ar.shape[1]), F32)
    for r in range(8):
        pw = pows[7 - r] if reverse else pows[r]
        tr = jnp.where(rows == r, pw[0], tr)
        ti = jnp.where(rows == r, pw[1], ti)
    return p1, p2, p4, p8, tr, ti


def _tile_scan(xr, xi, p1, p2, p4, reverse):
    rows = lax.broadcasted_iota(jnp.int32, xr.shape, 0)
    for s, (pr, pi) in ((1, p1), (2, p2), (4, p4)):
        if reverse:
            keep = rows < 8 - s
            sr, si = pltpu.roll(xr, 8 - s, 0), pltpu.roll(xi, 8 - s, 0)
        else:
            keep = rows >= s
            sr, si = pltpu.roll(xr, s, 0), pltpu.roll(xi, s, 0)
        sr, si = jnp.where(keep, sr, 0.0), jnp.where(keep, si, 0.0)
        mr, mi = _cmul(pr, pi, sr, si)
        xr, xi = xr + mr, xi + mi
    return xr, xi


def _s5_scan_fwd(bu, a, name, tb=512):
    t = bu.shape[0]
    cb = SCAN_CB
    nt = t // tb

    def body(b_ref, a_ref, x_ref, carry):
        @pl.when(pl.program_id(1) == 0)
        def _():
            carry[...] = jnp.zeros_like(carry)

        ar, ai = a_ref[:, 0:cb], a_ref[:, cb:2 * cb]
        p1, p2, p4, p8, tr, ti = _scan_tables(ar, ai, False)

        def step(j, c):
            cr, ci = c
            i = pl.multiple_of(j * 8, 8)
            xr, xi = _tile_scan(b_ref[pl.ds(i, 8), 0:cb], b_ref[pl.ds(i, 8), cb:2 * cb], p1, p2, p4, False)
            mr, mi = _cmul(tr, ti, cr, ci)
            xr, xi = xr + mr, xi + mi
            x_ref[pl.ds(i, 8), 0:cb] = xr
            x_ref[pl.ds(i, 8), cb:2 * cb] = xi
            return xr[7:8, :], xi[7:8, :]

        cr, ci = lax.fori_loop(0, tb // 8, step, (carry[0:1, :], carry[1:2, :]), unroll=2)
        carry[0:1, :] = cr
        carry[1:2, :] = ci

    blk = pl.BlockSpec((tb, 2 * cb), lambda j, i: (i, j))
    return pl.pallas_call(
        body, name=name, grid=(SSM_CH // cb, nt),
        in_specs=[blk, pl.BlockSpec((1, 2 * cb), lambda j, i: (0, j))], out_specs=blk,
        out_shape=jax.ShapeDtypeStruct((t, 2 * SSM_CH), F32), scratch_shapes=[pltpu.VMEM((8, cb), F32)],
        compiler_params=_params(("parallel", "arbitrary")))(bu, a)


def _s5_scan_bwd(dx, x, a, name, tb=512):
    t = dx.shape[0]
    cb = SCAN_CB
    nt = t // tb
    nj = tb // 8

    def body(d_ref, x_ref, xp_ref, a_ref, l_ref, da_ref, carry, acc):
        tblk = pl.program_id(1)

        @pl.when(tblk == 0)
        def _():
            carry[...] = jnp.zeros_like(carry)
            acc[...] = jnp.zeros_like(acc)

        ar, ai = a_ref[:, 0:cb], a_ref[:, cb:2 * cb]
        p1, p2, p4, p8, tr, ti = _scan_tables(ar, -ai, True)
        rows = lax.broadcasted_iota(jnp.int32, (8, cb), 0)

        def step(jj, c):
            cr, ci, sr_acc, si_acc = c
            j = nj - 1 - jj
            i = pl.multiple_of(j * 8, 8)
            lr, li = _tile_scan(d_ref[pl.ds(i, 8), 0:cb], d_ref[pl.ds(i, 8), cb:2 * cb], p1, p2, p4, True)
            mr, mi = _cmul(tr, ti, cr, ci)
            lr, li = lr + mr, li + mi
            l_ref[pl.ds(i, 8), 0:cb] = lr
            l_ref[pl.ds(i, 8), cb:2 * cb] = li
            ip = pl.multiple_of(jnp.maximum(j - 1, 0) * 8, 8)
            prev_r = jnp.where(j > 0, x_ref[pl.ds(ip, 8), 0:cb], xp_ref[:, 0:cb])
            prev_i = jnp.where(j > 0, x_ref[pl.ds(ip, 8), cb:2 * cb], xp_ref[:, cb:2 * cb])
            edge = jnp.where(jnp.logical_and(j == 0, tblk == nt - 1), 0.0, 1.0)
            xs_r = jnp.where(rows == 0, pltpu.roll(prev_r, 1, 0) * edge, pltpu.roll(x_ref[pl.ds(i, 8), 0:cb], 1, 0))
            xs_i = jnp.where(rows == 0, pltpu.roll(prev_i, 1, 0) * edge, pltpu.roll(x_ref[pl.ds(i, 8), cb:2 * cb], 1, 0))
            sr_acc = sr_acc + lr * xs_r + li * xs_i
            si_acc = si_acc + li * xs_r - lr * xs_i
            return lr[0:1, :], li[0:1, :], sr_acc, si_acc

        cr, ci, sr_acc, si_acc = lax.fori_loop(
            0, nj, step, (carry[0:1, :], carry[1:2, :], acc[:, 0:cb], acc[:, cb:2 * cb]))
        carry[0:1, :] = cr
        carry[1:2, :] = ci
        acc[:, 0:cb] = sr_acc
        acc[:, cb:2 * cb] = si_acc

        @pl.when(tblk == nt - 1)
        def _():
            da_ref[...] = jnp.sum(acc[...], axis=0, keepdims=True)

    blk = pl.BlockSpec((tb, 2 * cb), lambda j, i: (nt - 1 - i, j))
    prev = pl.BlockSpec((8, 2 * cb), lambda j, i: (jnp.maximum((nt - 1 - i) * (tb // 8) - 1, 0), j))
    vec = pl.BlockSpec((1, 2 * cb), lambda j, i: (0, j))
    return pl.pallas_call(
        body, name=name, grid=(SSM_CH // cb, nt), in_specs=[blk, blk, prev, vec], out_specs=[blk, vec],
        out_shape=[jax.ShapeDtypeStruct((t, 2 * SSM_CH), F32), jax.ShapeDtypeStruct((1, 2 * SSM_CH), F32)],
        scratch_shapes=[pltpu.VMEM((8, cb), F32), pltpu.VMEM((8, 2 * cb), F32)],
        compiler_params=_params(("parallel", "arbitrary")))(dx, x, x, a)


def _glu_fwd(yc, u, dvec, wg, bg, name, tr=256):
    t = yc.shape[0]

    def body(yc_ref, u_ref, d_ref, w_ref, b_ref, yl_ref, yb_ref):
        yl = yc_ref[...] + d_ref[...] * u_ref[...]
        yl_ref[...] = yl
        yg, _ = _gelu_and_grad(yl)
        z = jnp.dot(yg.astype(BF16), w_ref[...], preferred_element_type=F32) + b_ref[...]
        yb_ref[...] = (yg * _sigmoid(z)).astype(BF16)

    blk = pl.BlockSpec((tr, SSM_WIDTH), lambda i: (i, 0))
    vec = pl.BlockSpec((1, SSM_WIDTH), lambda i: (0, 0))
    return pl.pallas_call(
        body, name=name, grid=(t // tr,),
        in_specs=[blk, blk, vec, pl.BlockSpec((SSM_WIDTH, SSM_WIDTH), lambda i: (0, 0)), vec],
        out_specs=[blk, blk],
        out_shape=[jax.ShapeDtypeStruct((t, SSM_WIDTH), F32), jax.ShapeDtypeStruct((t, SSM_WIDTH), BF16)],
        compiler_params=_params(("parallel",)))(yc, u, dvec, wg, bg)


def _glu_bwd(yl, u, dvec, wg, bg, dyb, name, tr=256):
    t = yl.shape[0]

    def body(yl_ref, u_ref, d_ref, w_ref, b_ref, dy_ref, dyl_ref, du_ref, dw_ref, db_ref, dd_ref):
        @pl.when(pl.program_id(0) == 0)
        def _():
            dw_ref[...] = jnp.zeros_like(dw_ref)
            db_ref[...] = jnp.zeros_like(db_ref)
            dd_ref[...] = jnp.zeros_like(dd_ref)

        ylv, dyv, wv = yl_ref[...], dy_ref[...].astype(F32), w_ref[...]
        yg, dgelu = _gelu_and_grad(ylv)
        ygb = yg.astype(BF16)
        z = jnp.dot(ygb, wv, preferred_element_type=F32) + b_ref[...]
        sg = _sigmoid(z)
        dz = dyv * yg * sg * (1.0 - sg)
        dzb = dz.astype(BF16)
        dyg = dyv * sg + lax.dot_general(dzb, wv, (((1,), (1,)), ((), ())), preferred_element_type=F32)
        dyl = dyg * dgelu
        dyl_ref[...] = dyl.astype(BF16)
        du_ref[...] = dyl * d_ref[...]
        dw_ref[...] += lax.dot_general(ygb, dzb, (((0,), (0,)), ((), ())), preferred_element_type=F32)
        db_ref[...] += jnp.sum(dz, axis=0, keepdims=True)
        dd_ref[...] += jnp.sum(dyl * u_ref[...], axis=0, keepdims=True)

    blk = pl.BlockSpec((tr, SSM_WIDTH), lambda i: (i, 0))
    vec = pl.BlockSpec((1, SSM_WIDTH), lambda i: (0, 0))
    wsp = pl.BlockSpec((SSM_WIDTH, SSM_WIDTH), lambda i: (0, 0))
    return pl.pallas_call(
        body, name=name, grid=(t // tr,), in_specs=[blk, blk, vec, wsp, vec, blk],
        out_specs=[blk, blk, wsp, vec, vec],
        out_shape=[jax.ShapeDtypeStruct((t, SSM_WIDTH), BF16), jax.ShapeDtypeStruct((t, SSM_WIDTH), F32),
                   jax.ShapeDtypeStruct((SSM_WIDTH, SSM_WIDTH), F32), jax.ShapeDtypeStruct((1, SSM_WIDTH), F32),
                   jax.ShapeDtypeStruct((1, SSM_WIDTH), F32)],
        compiler_params=_params(("arbitrary",)))(yl, u, dvec, wg, bg, dyb)


def _mesh_pos():
    return lax.axis_index("x"), lax.axis_index("y"), lax.axis_index("c")


def _all_gather(arrays, name):
    na = len(arrays)

    def body(*refs):
        ins, outs = refs[:na], refs[na:2 * na]
        send_sems, recv_sems, local_sems = refs[2 * na:]
        x, y, c = _mesh_pos()
        me, sibling = (x, y, c), (x, y, 1 - c)
        chips = [(1 - x, y), (x, 1 - y), (1 - x, 1 - y)]
        waits = []
        for ai in range(na):
            in_ref, out_ref = ins[ai], outs[ai]

            def slot(px, py, pc, out_ref=out_ref):
                return out_ref.at[4 * px + 2 * py + pc]

            def copy(kk, block, to, src=None, ai=ai, slot=slot):
                return pltpu.make_async_remote_copy(
                    src_ref=slot(*block) if src is None else src, dst_ref=slot(*block),
                    send_sem=send_sems.at[ai, kk], recv_sem=recv_sems.at[ai, kk], device_id=to, device_id_type=MESH)

            mine = pltpu.make_async_copy(in_ref, slot(*me), local_sems.at[ai])
            mine.start()
            first = [copy(0, me, sibling, src=in_ref)]
            first += [copy(1 + j, me, (*chip, c), src=in_ref) for j, chip in enumerate(chips)]
            for cp in first:
                cp.start()
            waits.append((copy, mine, first))
        sends = []
        for ai in range(na):
            copy, mine, first = waits[ai]
            passed = [copy(4 + j, (*chip, c), sibling) for j, chip in enumerate(chips)]
            for j, chip in enumerate(chips):
                copy(1 + j, (*chip, c), me).wait_recv()
                passed[j].start()
            sends.append(passed)
        for ai in range(na):
            copy, mine, first = waits[ai]
            copy(0, sibling, me).wait_recv()
            for j, chip in enumerate(chips):
                copy(4 + j, (*chip, 1 - c), me).wait_recv()
            for cp in first + sends[ai]:
                cp.wait_send()
            mine.wait()

    any_spec = pl.BlockSpec(memory_space=pl.ANY)
    return pl.pallas_call(
        body, name=name, in_specs=[any_spec] * na, out_specs=[any_spec] * na,
        out_shape=[jax.ShapeDtypeStruct((N_DEV,) + a.shape, a.dtype) for a in arrays],
        scratch_shapes=[pltpu.SemaphoreType.DMA((na, 7)), pltpu.SemaphoreType.DMA((na, 7)),
                        pltpu.SemaphoreType.DMA((na,))],
        compiler_params=pltpu.CompilerParams(has_side_effects=True))(*arrays)


def _swap_sibling(arrays, name):
    na = len(arrays)
    offs = np.concatenate([[0], np.cumsum([a.shape[1] for a in arrays])]).astype(int)
    rows = int(offs[-1])
    dtype = arrays[0].dtype

    def body(*refs):
        ins, keep_ref, recv_ref = refs[:na], refs[na], refs[na + 1]
        send_sems, recv_sems, local_sems = refs[na + 2:]
        x, y, c = _mesh_pos()
        started = []
        for ai in range(na):
            span = pl.ds(int(offs[ai]), arrays[ai].shape[1])
            for k in range(4):
                local = pltpu.make_async_copy(ins[ai].at[2 * k + c], keep_ref.at[k, span], local_sems.at[ai, k])
                remote = pltpu.make_async_remote_copy(
                    src_ref=ins[ai].at[2 * k + 1 - c], dst_ref=recv_ref.at[k, span], send_sem=send_sems.at[ai, k],
                    recv_sem=recv_sems.at[ai, k], device_id=(x, y, 1 - c), device_id_type=MESH)
                local.start()
                remote.start()
                started.append((local, remote))
        for local, remote in started:
            remote.wait()
            local.wait()

    any_spec = pl.BlockSpec(memory_space=pl.ANY)
    return pl.pallas_call(
        body, name=name, in_specs=[any_spec] * na, out_specs=[any_spec, any_spec],
        out_shape=[jax.ShapeDtypeStruct((4, rows, PACK_COLS), dtype)] * 2,
        scratch_shapes=[pltpu.SemaphoreType.DMA((na, 4)), pltpu.SemaphoreType.DMA((na, 4)),
                        pltpu.SemaphoreType.DMA((na, 4))])(*arrays)


def _exchange_chips(send, name):
    def body(s_ref, o_ref, send_sems, recv_sems, local_sem):
        x, y, c = _mesh_pos()
        my_chip = 2 * x + y
        chips = [(1 - x, y), (x, 1 - y), (1 - x, 1 - y)]
        mine = pltpu.make_async_copy(s_ref.at[my_chip], o_ref.at[my_chip], local_sem)
        mine.start()
        cps = [pltpu.make_async_remote_copy(
            src_ref=s_ref.at[2 * cx + cy], dst_ref=o_ref.at[my_chip], send_sem=send_sems.at[j], recv_sem=recv_sems.at[j],
            device_id=(cx, cy, c), device_id_type=MESH) for j, (cx, cy) in enumerate(chips)]
        for cp in cps:
            cp.start()
        for j, (cx, cy) in enumerate(chips):
            pltpu.make_async_remote_copy(
                src_ref=s_ref.at[my_chip], dst_ref=o_ref.at[2 * cx + cy], send_sem=send_sems.at[j],
                recv_sem=recv_sems.at[j], device_id=(cx, cy, c), device_id_type=MESH).wait_recv()
        for cp in cps:
            cp.wait_send()
        mine.wait()

    any_spec = pl.BlockSpec(memory_space=pl.ANY)
    return pl.pallas_call(
        body, name=name, in_specs=[any_spec], out_specs=any_spec,
        out_shape=jax.ShapeDtypeStruct(send.shape, send.dtype),
        scratch_shapes=[pltpu.SemaphoreType.DMA((3,)), pltpu.SemaphoreType.DMA((3,)), pltpu.SemaphoreType.DMA],
        compiler_params=pltpu.CompilerParams(has_side_effects=True))(send)


def _pair_sum(keep, recv, name, tr=464):
    nchip, rows, cols = keep.shape

    def body(g_ref, r_ref, o_ref):
        o_ref[...] = (g_ref[...].astype(F32) + r_ref[...].astype(F32)).astype(BF16)

    blk = pl.BlockSpec((1, tr, cols), lambda k, i: (k, i, 0))
    return pl.pallas_call(
        body, name=name, grid=(nchip, rows // tr), in_specs=[blk, blk], out_specs=blk,
        out_shape=jax.ShapeDtypeStruct((nchip, rows, cols), BF16),
        compiler_params=_params(("parallel", "parallel")))(keep, recv)


def _sum_leading(parts, name, tr=464):
    nparts, rows, cols = parts.shape
    tr = tr if rows % tr == 0 else rows

    def body(p_ref, o_ref):
        acc = p_ref[0].astype(F32)
        for i in range(1, nparts):
            acc = acc + p_ref[i].astype(F32)
        o_ref[...] = acc

    return pl.pallas_call(
        body, name=name, grid=(rows // tr,),
        in_specs=[pl.BlockSpec((nparts, tr, cols), lambda i: (0, i, 0))],
        out_specs=pl.BlockSpec((tr, cols), lambda i: (i, 0)), out_shape=jax.ShapeDtypeStruct((rows, cols), F32),
        compiler_params=_params(("parallel",)))(parts)


def _adamw(w, g, m, v, name):
    shape = w.shape
    cols = shape[-1]
    rows = int(np.prod(shape[:-1])) if len(shape) > 1 else 1
    w2, g2, m2, v2 = (a.reshape(rows, cols) for a in (w, g, m, v))
    tr = rows
    for cand in (512, 256, 128, 64, 32, 16, 8):
        if rows % cand == 0 and rows > cand:
            tr = cand
            break
    bc1, bc2 = 1.0 - ADAM_B1 ** ADAM_STEP, 1.0 - ADAM_B2 ** ADAM_STEP

    def body(w_ref, g_ref, m_ref, v_ref, d_ref, nm_ref, nv_ref):
        gv = g_ref[...]
        nm = ADAM_B1 * m_ref[...] + (1.0 - ADAM_B1) * gv
        nv = ADAM_B2 * v_ref[...] + (1.0 - ADAM_B2) * (gv * gv)
        nm_ref[...] = nm
        nv_ref[...] = nv
        d_ref[...] = -ADAM_LR * ((nm / bc1) / (jnp.sqrt(nv / bc2) + ADAM_EPS) + ADAM_WD * w_ref[...])

    blk = pl.BlockSpec((tr, cols), lambda i: (i, 0))
    outs = pl.pallas_call(
        body, name=name, grid=(rows // tr,), in_specs=[blk] * 4, out_specs=[blk] * 3,
        out_shape=[jax.ShapeDtypeStruct((rows, cols), F32)] * 3, compiler_params=_params(("parallel",)))(w2, g2, m2, v2)
    return tuple(o.reshape(shape) for o in outs)


WEIGHT_NAMES = ['norm_mix_g', 'norm_xa_g', 'norm_ffn_g', 'norm_mem_g', 'norm_final_g', 'w_in_ab', 'conv_qkv_a',
                'a_log_a', 'dt_bias_a', 'onorm_g_a', 'ssm_lambda_re', 'ssm_lambda_im', 'ssm_b_re', 'ssm_b_im',
                'ssm_c_re', 'ssm_c_im', 'ssm_d', 'ssm_log_dt', 'w_glu_b', 'b_glu_b', 'w_out_ab', 'pool_w',
                'pool_scale', 'xa_wq', 'xa_wkv', 'xa_wo', 'ffn_w_up', 'ffn_conv', 'ffn_w_down']
BIG_SHARDED = {'w_in_ab': ((1, 1024, 2568), 2), 'w_glu_b': ((1, 512, 512), 1), 'w_out_ab': ((1, 1024, 1024), 1),
               'pool_w': ((1, 4, 256, 256), 2), 'xa_wq': ((2, 1024, 1024), 1), 'xa_wkv': ((2, 1024, 2048), 2),
               'xa_wo': ((2, 1024, 1024), 1), 'ffn_w_up': ((2, 1024, 5632), 2), 'ffn_w_down': ((2, 2816, 1024), 1)}
SMALL_SHARDED = {'conv_qkv_a': ((1, 4, 1536), 2), 'pool_scale': ((1, 1024), 1), 'ffn_conv': ((2, 3, 5632), 2)}
REPLICATED = {'norm_mix_g': (2, 1024), 'norm_xa_g': (2, 1024), 'norm_ffn_g': (2, 1024), 'norm_mem_g': (1024,),
              'norm_final_g': (1024,), 'a_log_a': (1, 4), 'dt_bias_a': (1, 4), 'onorm_g_a': (1, 128),
              'ssm_lambda_re': (1, 32, 64), 'ssm_lambda_im': (1, 32, 64), 'ssm_b_re': (1, 32, 64, 16),
              'ssm_b_im': (1, 32, 64, 16), 'ssm_c_re': (1, 32, 16, 64), 'ssm_c_im': (1, 32, 16, 64),
              'ssm_d': (1, 32, 16), 'ssm_log_dt': (1, 32), 'b_glu_b': (1, 512)}
BIG_ROW_ALIGN = 464 * 8


def _shard_shape(shape, axis):
    return tuple(s // N_DEV if i == axis else s for i, s in enumerate(shape))


def _pad_rows(a, rows):
    return a if a.shape[0] == rows else jnp.concatenate([a, jnp.zeros((rows - a.shape[0],) + a.shape[1:], a.dtype)], 0)


def _round_up(n, m):
    return (n + m - 1) // m * m


def _pack_rows(flat_list, align):
    rows = []
    for a in flat_list:
        n = a.shape[0]
        r = _round_up(n, PACK_COLS) // PACK_COLS
        if n != r * PACK_COLS:
            a = jnp.concatenate([a, jnp.zeros((r * PACK_COLS - n,), a.dtype)])
        rows.append(a.reshape(r, PACK_COLS))
    out = jnp.concatenate(rows, 0)
    return _pad_rows(out, _round_up(out.shape[0], align))


def _row_offsets(sizes):
    offs, r = [], 0
    for n in sizes:
        offs.append(r)
        r += _round_up(n, PACK_COLS) // PACK_COLS
    return offs


def _split_shards(full, axis):
    shape = full.shape
    s = shape[axis] // N_DEV
    a = full.reshape(shape[:axis] + (N_DEV, s) + shape[axis + 1:])
    return jnp.moveaxis(a, axis, 0).reshape(N_DEV, -1)


def _merge_shards(pieces, shape, axis):
    sh = _shard_shape(shape, axis)
    a = pieces.reshape((N_DEV,) + sh)
    a = jnp.moveaxis(a, 0, axis)
    return a.reshape(shape)


_SCAN_NB = SSM_CH // SCAN_CB


def _to_scan_layout(m, axis):
    shape = m.shape
    m = m.reshape(shape[:axis] + (2, _SCAN_NB, SCAN_CB) + shape[axis + 1:])
    return jnp.swapaxes(m, axis, axis + 1).reshape(shape)


def _from_scan_layout(m, axis):
    shape = m.shape
    m = m.reshape(shape[:axis] + (_SCAN_NB, 2, SCAN_CB) + shape[axis + 1:])
    return jnp.swapaxes(m, axis, axis + 1).reshape(shape)


def _s5_discretise(lam_re, lam_im, b_re, b_im, log_dt):
    dt = jnp.exp(log_dt)[:, None]
    mag = jnp.exp(lam_re * dt)
    ang = lam_im * dt
    lb_re, lb_im = mag * jnp.cos(ang), mag * jnp.sin(ang)
    den = lam_re * lam_re + lam_im * lam_im
    nr, ni = lb_re - 1.0, lb_im
    coef_re = (nr * lam_re + ni * lam_im) / den
    coef_im = (ni * lam_re - nr * lam_im) / den
    bb_re = coef_re[..., None] * b_re - coef_im[..., None] * b_im
    bb_im = coef_re[..., None] * b_im + coef_im[..., None] * b_re
    return lb_re, lb_im, bb_re, bb_im


def _s5_matrices(lb_re, lb_im, bb_re, bb_im, c_re, c_im):
    eye = jnp.eye(N_GROUPS, dtype=F32)
    bmat = lambda bb: jnp.einsum('gph,gk->ghkp', bb, eye).reshape(SSM_WIDTH, SSM_CH)
    cmat = lambda cc: jnp.einsum('ghp,gk->gpkh', cc, eye).reshape(SSM_CH, SSM_WIDTH)
    b_in = _to_scan_layout(jnp.concatenate([bmat(bb_re), bmat(bb_im)], axis=1), 1)
    c_out = _to_scan_layout(jnp.concatenate([cmat(c_re), -cmat(c_im)], axis=0), 0)
    a_row = _to_scan_layout(jnp.concatenate([lb_re.reshape(1, SSM_CH), lb_im.reshape(1, SSM_CH)], axis=1), 1)
    return b_in, c_out, a_row


def _s5_matrix_grads(db_in, dc_out, da_row):
    db_nat = _from_scan_layout(db_in, 1)
    dc_nat = _from_scan_layout(dc_out, 0)
    da_nat = _from_scan_layout(da_row, 1)
    eye = jnp.eye(N_GROUPS, dtype=F32)
    bgrad = lambda m: jnp.einsum('ghkp,gk->gph', m.reshape(N_GROUPS, SSM_GROUP, N_GROUPS, SSM_STATE), eye)
    cgrad = lambda m: jnp.einsum('gpkh,gk->ghp', m.reshape(N_GROUPS, SSM_STATE, N_GROUPS, SSM_GROUP), eye)
    dbb_re, dbb_im = bgrad(db_nat[:, :SSM_CH]), bgrad(db_nat[:, SSM_CH:])
    dc_re, dc_im = cgrad(dc_nat[:SSM_CH]), -cgrad(dc_nat[SSM_CH:])
    dlb_re = da_nat[0, :SSM_CH].reshape(N_GROUPS, SSM_STATE)
    dlb_im = da_nat[0, SSM_CH:].reshape(N_GROUPS, SSM_STATE)
    return dlb_re, dlb_im, dbb_re, dbb_im, dc_re, dc_im


def _hybrid_fwd(xn, x, wts, p):
    sv = {}
    hq = _mm(xn, wts['w_qkv_t'], "nt", "l0_in_qkv")
    gate = _mm(xn, wts['w_gate_t'], "nt", "l0_in_gate")
    ba = _mm(xn, wts['w_ba_t'], "nt", "l0_in_ba")
    u = _mm(xn, wts['w_u_t'], "nt", "l0_in_u")
    conv = p['conv_qkv']
    q = _qkv_pre_fwd(hq, conv, 0, 4, True, HEAD_A ** -0.5, "l0_q_pre")
    k = _qkv_pre_fwd(hq, conv, 4, 4, True, 1.0, "l0_k_pre")
    v = _qkv_pre_fwd(hq, conv, 8, 4, False, 1.0, "l0_v_pre")
    gates = _gates_fwd(ba, p['arow'], p['brow'], "l0_gates")
    gb = jnp.stack([gates[:, 0:4].T, gates[:, 4:8].T], axis=-1)
    o, tm_all, s_all = _gdr_fwd(q, k, v, gb, "l0_gdr_fwd")
    y_a = _onorm_fwd(o, gate, p['onorm_g'], "l0_onorm")
    bu = _mm(u, p['b_in'], "nn", "l0_s5_bu")
    xs = _s5_scan_fwd(bu, p['a_row'], "l0_s5_scan")
    yc = _mm(xs, p['c_out'], "nn", "l0_s5_cx")
    yl, y_b = _glu_fwd(yc, u, p['d_row'], wts['w_glu'], p['b_glu'], "l0_glu")
    mixed = jnp.concatenate([y_a, y_b], axis=1)
    x1 = _mm(mixed, wts['w_out'], "nn", "l0_out", res=x)
    sv.update(hq=hq, gate=gate, ba=ba, u=u, q=q, k=k, v=v, gb=gb, o=o, tm=tm_all, s=s_all, xs=xs, yl=yl, mixed=mixed)
    return x1, sv


def _hybrid_bwd(dx1, xn, wts, p, sv):
    gr = {}
    dmixed = _mm(dx1, wts['w_out'], "nt", "l0_out_dx", out_dtype=BF16)
    gr['w_out_ab'] = _mm(sv['mixed'], dx1, "tn", "l0_out_dw", out_dtype=BF16)
    dya, dyb = dmixed[:, :WIDTH_A], dmixed[:, WIDTH_A:]
    dyl, du_direct, gr['w_glu_b'], gr['b_glu_b'], dd = _glu_bwd(
        sv['yl'], sv['u'], p['d_row'], wts['w_glu'], p['b_glu'], dyb, "l0_glu_bwd")
    dxs = _mm(dyl, p['c_out'], "nt", "l0_s5_cx_dx")
    dc_out = _mm(sv['xs'], dyl, "tn", "l0_s5_cx_dw")
    lam, da_row = _s5_scan_bwd(dxs, sv['xs'], p['a_row'], "l0_s5_scan_bwd")
    du = _mm(lam, p['b_in'], "nt", "l0_s5_bu_dx", res=du_direct, out_dtype=BF16)
    db_in = _mm(sv['u'], lam, "tn", "l0_s5_bu_dw")
    gr['s5'] = (db_in, dc_out, da_row, dd)
    do, dgate, gr['onorm_g_a'] = _onorm_bwd(sv['o'], sv['gate'], p['onorm_g'], dya, "l0_onorm_bwd")
    dq, dk, dv, dgb = _gdr_bwd(sv['q'], sv['k'], sv['v'], sv['gb'], sv['tm'], sv['s'], do, "l0_gdr_bwd")
    conv = p['conv_qkv']
    dhq_q, dcw_q = _qkv_pre_bwd(sv['hq'], conv, dq, 0, 4, True, HEAD_A ** -0.5, "l0_q_pre_bwd")
    dhq_k, dcw_k = _qkv_pre_bwd(sv['hq'], conv, dk, 4, 4, True, 1.0, "l0_k_pre_bwd")
    dhq_v, dcw_v = _qkv_pre_bwd(sv['hq'], conv, dv, 8, 4, False, 1.0, "l0_v_pre_bwd")
    gr['conv_qkv_a'] = jnp.concatenate([dcw_q, dcw_k, dcw_v], axis=1)
    dhq = jnp.concatenate([dhq_q, dhq_k, dhq_v], axis=1)
    dgates = jnp.concatenate([dgb[:, :, 0].T, dgb[:, :, 1].T, jnp.zeros((SEQ, LANE - 8), F32)], axis=1)
    dba, da_log, ddt_bias = _gates_bwd(sv['ba'], p['arow'], p['brow'], dgates, "l0_gates_bwd")
    gr['a_log_a'], gr['dt_bias_a'] = da_log[:, 4:8], ddt_bias[:, 4:8]
    dxn = _mm(dhq, wts['w_qkv_t'], "nn", "l0_in_qkv_dx")
    dxn = _mm(dgate, wts['w_gate_t'], "nn", "l0_in_gate_dx", res=dxn)
    dxn = _mm(dba, wts['w_ba_t'], "nn", "l0_in_ba_dx", res=dxn)
    dxn = _mm(du, wts['w_u_t'], "nn", "l0_in_u_dx", res=dxn)
    dw_qkv_t = _mm(dhq, xn, "tn", "l0_in_qkv_dw", out_dtype=BF16)
    dw_gate_t = _mm(dgate, xn, "tn", "l0_in_gate_dw", out_dtype=BF16)
    dw_ba_t = _mm(dba, xn, "tn", "l0_in_ba_dw", out_dtype=BF16)
    dw_u_t = _mm(du, xn, "tn", "l0_in_u_dw", out_dtype=BF16)
    gr['w_in_t'] = jnp.concatenate([dw_qkv_t, dw_gate_t, dw_ba_t[:8], dw_u_t], axis=0)
    return dxn, gr


def _xa_fwd(x1, g, mem_n, wq, wkv_t, wo, tag):
    xq = _rms_fwd(x1, g, BF16, tag + "_norm")
    q = _mm(xq, wq, "nn", tag + "_q", out_dtype=BF16)
    kv = _mm(mem_n, wkv_t, "nt", tag + "_kv", out_dtype=BF16)
    o = _attn_fwd(q, kv, tag + "_attn")
    x2 = _mm(o, wo, "nn", tag + "_o", res=x1)
    return x2, dict(xq=xq, q=q, kv=kv, o=o)


def _xa_bwd(dx2, x1, g, mem_n, wq, wkv_t, wo, sv, tag):
    do = _mm(dx2, wo, "nt", tag + "_o_dx", out_dtype=BF16)
    dwo = _mm(sv['o'], dx2, "tn", tag + "_o_dw", out_dtype=BF16)
    dq, dk, dv = _attn_bwd(sv['q'], sv['kv'], do, tag + "_attn_bwd")
    dkv = jnp.concatenate([dk, dv], axis=1).astype(BF16)
    dxq = _mm(dq, wq, "nt", tag + "_q_dx")
    dwq = _mm(sv['xq'], dq, "tn", tag + "_q_dw", out_dtype=BF16)
    dmem_n = _mm(dkv, wkv_t, "nn", tag + "_kv_dx")
    dwkv_t = _mm(dkv, mem_n, "tn", tag + "_kv_dw", out_dtype=BF16)
    dx1, dg = _rms_bwd(x1, g, dxq, dx2, tag + "_norm_bwd")
    return dx1, dmem_n, dict(wq=dwq, wkv_t=dwkv_t, wo=dwo, g=dg)


def _ffn_fwd(x2, g, w_up_t, conv, w_down, tag):
    xf = _rms_fwd(x2, g, BF16, tag + "_norm")
    h = _mm(xf, w_up_t, "nt", tag + "_up")
    a = _ffn_act_fwd(h, conv, tag + "_act")
    x3 = _mm(a, w_down, "nn", tag + "_down", res=x2)
    return x3, dict(xf=xf, h=h, a=a)


def _ffn_bwd(dx3, x2, g, w_up_t, conv, w_down, sv, tag):
    da = _mm(dx3, w_down, "nt", tag + "_down_dx")
    dw_down = _mm(sv['a'], dx3, "tn", tag + "_down_dw", out_dtype=BF16)
    dh, dconv = _ffn_act_bwd(sv['h'], conv, da, tag + "_act_bwd")
    dxf = _mm(dh, w_up_t, "nn", tag + "_up_dx")
    dw_up_t = _mm(dh, sv['xf'], "tn", tag + "_up_dw", out_dtype=BF16)
    dx2, dg = _rms_bwd(x2, g, dxf, dx3, tag + "_norm_bwd")
    return dx2, dict(w_up_t=dw_up_t, conv=dconv, w_down=dw_down, g=dg)


BIG_NAMES, SMALL_NAMES, REP_NAMES = list(BIG_SHARDED), list(SMALL_SHARDED), list(REPLICATED)
BIG_SIZES = [int(np.prod(_shard_shape(*BIG_SHARDED[n]))) for n in BIG_NAMES]
SMALL_SIZES = [int(np.prod(_shard_shape(*SMALL_SHARDED[n]))) for n in SMALL_NAMES]


PIECES = [('w_in_t', 384), ('w_glu', 32), ('w_out', 128), ('pool_w', 32), ('wq0', 128), ('wq1', 128),
          ('wkv_t0', 256), ('wkv_t1', 256), ('wo0', 128), ('wo1', 128), ('up_t0', 704), ('up_t1', 704),
          ('down0', 352), ('down1', 352)]
PIECE_OFFS = dict(zip([k for k, _ in PIECES], np.concatenate([[0], np.cumsum([r for _, r in PIECES])[:-1]]).tolist()))
W_IN_ROWS = 4 * WIDTH_A + 2 * N_HEADS_A + SSM_WIDTH
W_IN_PIECE = W_IN_ROWS // N_DEV


def _gather_weights(inp):
    bf = lambda a: a.astype(BF16)
    local = {'w_in_t': bf(inp['w_in_ab'][0]).T, 'w_glu': bf(inp['w_glu_b'][0]), 'w_out': bf(inp['w_out_ab'][0]),
             'pool_w': bf(inp['pool_w'][0])}
    for l in range(2):
        local['wq%d' % l] = bf(inp['xa_wq'][l])
        local['wkv_t%d' % l] = bf(inp['xa_wkv'][l]).T
        local['wo%d' % l] = bf(inp['xa_wo'][l])
        local['up_t%d' % l] = bf(inp['ffn_w_up'][l]).T
        local['down%d' % l] = bf(inp['ffn_w_down'][l])
    small_offs = _row_offsets(SMALL_SIZES)
    small_local = _pack_rows([inp[n].reshape(-1) for n in SMALL_NAMES], 8)
    keys = [k for k, _ in PIECES]
    gathered = _all_gather([local[k] for k in keys] + [small_local], "gather_weights")
    got = dict(zip(keys, gathered[:-1]))
    small_all = gathered[-1]
    rows = lambda a: a.reshape(N_DEV * a.shape[1], a.shape[2])
    full = {'w_in_t': rows(got['w_in_t']), 'w_glu': rows(got['w_glu']), 'w_out': rows(got['w_out']),
            'pool_w': jnp.swapaxes(got['pool_w'], 0, 1).reshape(len(POOL_WINDOWS), POOL_GROUP, POOL_GROUP)}
    for name in ('wq', 'wkv_t', 'wo', 'up_t', 'down'):
        full[name] = [rows(got[name + '0']), rows(got[name + '1'])]
    for n, off, size in zip(SMALL_NAMES, small_offs, SMALL_SIZES):
        r = _round_up(size, PACK_COLS) // PACK_COLS
        full[n] = _merge_shards(small_all[:, off:off + r].reshape(N_DEV, -1)[:, :size], *SMALL_SHARDED[n])
    return full


def _local_step(inp, full):
    f32_of = lambda n: inp[n].astype(F32)
    w_in_t = full['w_in_t']
    wts0 = dict(w_qkv_t=w_in_t[:3 * WIDTH_A], w_gate_t=w_in_t[3 * WIDTH_A:4 * WIDTH_A],
                w_ba_t=jnp.concatenate([w_in_t[4 * WIDTH_A:4 * WIDTH_A + 8], jnp.zeros((LANE - 8, D_MODEL), BF16)], 0),
                w_u_t=w_in_t[4 * WIDTH_A + 8:], w_glu=full['w_glu'], w_out=full['w_out'])
    lb_disc, disc_vjp = jax.vjp(_s5_discretise, f32_of('ssm_lambda_re')[0], f32_of('ssm_lambda_im')[0],
                                f32_of('ssm_b_re')[0], f32_of('ssm_b_im')[0], f32_of('ssm_log_dt')[0])
    b_in, c_out, a_row = _s5_matrices(*lb_disc, f32_of('ssm_c_re')[0], f32_of('ssm_c_im')[0])
    zeros4 = jnp.zeros((1, 4), F32)
    p0 = dict(conv_qkv=full['conv_qkv_a'][0], onorm_g=f32_of('onorm_g_a'),
              arow=jnp.concatenate([zeros4, f32_of('a_log_a'), jnp.zeros((1, LANE - 8), F32)], 1),
              brow=jnp.concatenate([zeros4, f32_of('dt_bias_a'), jnp.zeros((1, LANE - 8), F32)], 1),
              b_in=b_in.astype(BF16), c_out=c_out.astype(BF16), a_row=a_row,
              d_row=f32_of('ssm_d').reshape(1, SSM_WIDTH), b_glu=f32_of('b_glu_b'))

    x0 = inp['x'][0]
    mem_n = _rms_fwd(inp['mem'][0], inp['norm_mem_g'], BF16, "mem_norm")
    xn0 = _rms_fwd(x0, inp['norm_mix_g'][0], BF16, "l0_mix_norm")
    x1, sv_mix0 = _hybrid_fwd(xn0, x0, wts0, p0)
    x2, sv_xa0 = _xa_fwd(x1, inp['norm_xa_g'][0], mem_n, full['wq'][0], full['wkv_t'][0], full['wo'][0], "l0_xa")
    x3, sv_ffn0 = _ffn_fwd(x2, inp['norm_ffn_g'][0], full['up_t'][0], full['ffn_conv'][0], full['down'][0], "l0_ffn")
    xn1 = _rms_fwd(x3, inp['norm_mix_g'][1], F32, "l1_mix_norm")
    x4 = _pool_fwd(xn1, full['pool_w'], full['pool_scale'], x3, "l1_pool")
    x5, sv_xa1 = _xa_fwd(x4, inp['norm_xa_g'][1], mem_n, full['wq'][1], full['wkv_t'][1], full['wo'][1], "l1_xa")
    x6, sv_ffn1 = _ffn_fwd(x5, inp['norm_ffn_g'][1], full['up_t'][1], full['ffn_conv'][1], full['down'][1], "l1_ffn")
    loss_part, dx6, dg_final = _loss_head(x6, inp['norm_final_g'], inp['loss_target'][0], "loss_head")

    dx5, g_ffn1 = _ffn_bwd(dx6, x5, inp['norm_ffn_g'][1], full['up_t'][1], full['ffn_conv'][1], full['down'][1], sv_ffn1, "l1_ffn")
    dx4, dmem1, g_xa1 = _xa_bwd(dx5, x4, inp['norm_xa_g'][1], mem_n, full['wq'][1], full['wkv_t'][1], full['wo'][1], sv_xa1, "l1_xa")
    dxn1, dpool_w, dpool_scale = _pool_bwd(xn1, full['pool_w'], full['pool_scale'], dx4, "l1_pool_bwd")
    dx3, dg_mix1 = _rms_bwd(x3, inp['norm_mix_g'][1], dxn1, dx4, "l1_mix_norm_bwd")
    dx2, g_ffn0 = _ffn_bwd(dx3, x2, inp['norm_ffn_g'][0], full['up_t'][0], full['ffn_conv'][0], full['down'][0], sv_ffn0, "l0_ffn")
    dx1, dmem0, g_xa0 = _xa_bwd(dx2, x1, inp['norm_xa_g'][0], mem_n, full['wq'][0], full['wkv_t'][0], full['wo'][0], sv_xa0, "l0_xa")
    dxn0, g_mix0 = _hybrid_bwd(dx1, xn0, wts0, p0, sv_mix0)
    grad_x, dg_mix0 = _rms_bwd(x0, inp['norm_mix_g'][0], dxn0, dx1, "l0_mix_norm_bwd")
    _, dg_mem = _rms_bwd(inp['mem'][0], inp['norm_mem_g'], dmem0 + dmem1, None, "mem_norm_bwd")

    db_in, dc_out, da_row, dd = g_mix0['s5']
    dlb_re, dlb_im, dbb_re, dbb_im, dc_re, dc_im = _s5_matrix_grads(db_in, dc_out, da_row)
    dlam_re, dlam_im, dbr, dbi, dlog_dt = disc_vjp((dlb_re, dlb_im, dbb_re, dbb_im))

    pieces = lambda a: a.reshape(N_DEV, a.shape[0] // N_DEV, a.shape[1])
    w_in_pieces = pieces(g_mix0['w_in_t'])
    w_in_pieces = jnp.concatenate(
        [w_in_pieces, jnp.zeros((N_DEV, PIECES[0][1] - W_IN_PIECE, D_MODEL), BF16)], axis=1)
    pool_pieces = jnp.swapaxes(dpool_w.astype(BF16).reshape(len(POOL_WINDOWS), N_DEV, -1, POOL_GROUP), 0, 1)
    big_grads = {
        'w_in_t': w_in_pieces, 'w_glu': g_mix0['w_glu_b'].astype(BF16).reshape(N_DEV, -1, PACK_COLS),
        'w_out': pieces(g_mix0['w_out_ab']), 'pool_w': pool_pieces.reshape(N_DEV, -1, PACK_COLS)}
    for l, (gx, gf) in enumerate(((g_xa0, g_ffn0), (g_xa1, g_ffn1))):
        big_grads['wq%d' % l], big_grads['wkv_t%d' % l] = pieces(gx['wq']), pieces(gx['wkv_t'])
        big_grads['wo%d' % l] = pieces(gx['wo'])
        big_grads['up_t%d' % l], big_grads['down%d' % l] = pieces(gf['w_up_t']), pieces(gf['w_down'])
    rep_grads = {
        'norm_mix_g': jnp.concatenate([dg_mix0, dg_mix1], 0), 'norm_xa_g': jnp.concatenate([g_xa0['g'], g_xa1['g']], 0),
        'norm_ffn_g': jnp.concatenate([g_ffn0['g'], g_ffn1['g']], 0), 'norm_mem_g': dg_mem.reshape(-1),
        'norm_final_g': dg_final.reshape(-1), 'a_log_a': g_mix0['a_log_a'], 'dt_bias_a': g_mix0['dt_bias_a'],
        'onorm_g_a': g_mix0['onorm_g_a'], 'ssm_lambda_re': dlam_re[None], 'ssm_lambda_im': dlam_im[None],
        'ssm_b_re': dbr[None], 'ssm_b_im': dbi[None], 'ssm_c_re': dc_re[None], 'ssm_c_im': dc_im[None],
        'ssm_d': dd.reshape(1, N_GROUPS, SSM_GROUP), 'ssm_log_dt': dlog_dt[None], 'b_glu_b': g_mix0['b_glu_b']}
    small_grads = {'conv_qkv_a': g_mix0['conv_qkv_a'][None], 'pool_scale': dpool_scale,
                   'ffn_conv': jnp.stack([g_ffn0['conv'], g_ffn1['conv']])}
    return loss_part, grad_x, big_grads, rep_grads, small_grads


def _reduce_gradients(big_grads, rep_grads, small_grads):
    for key, rows in PIECES:
        assert big_grads[key].shape == (N_DEV, rows, PACK_COLS) and big_grads[key].dtype == BF16, key
    keep, from_sibling = _swap_sibling([big_grads[k] for k, _ in PIECES], "grads_to_sibling")
    chip_sums = _pair_sum(keep, from_sibling, "grads_pair_sum")
    from_chips = _exchange_chips(chip_sums, "grads_to_chips")
    big_reduced = _sum_leading(from_chips, "grads_chip_sum")

    misc_list = [rep_grads[n].astype(F32).reshape(-1) for n in REP_NAMES] + \
                [small_grads[n].astype(F32).reshape(-1) for n in SMALL_NAMES]
    misc_sizes = [int(a.shape[0]) for a in misc_list]
    misc_local = _pack_rows(misc_list, 8)
    (misc_all,) = _all_gather([misc_local], "gather_small_grads")
    misc_sum = _sum_leading(misc_all, "small_grads_sum")
    return big_reduced, misc_sum, misc_sizes


def _update(inp, loss_part, grad_x, big_reduced, misc_sum, misc_sizes):
    big_names, small_names, rep_names = BIG_NAMES, SMALL_NAMES, REP_NAMES
    misc_offs = _row_offsets(misc_sizes)
    dev = 4 * lax.axis_index("x") + 2 * lax.axis_index("y") + lax.axis_index("c")
    piece_rows = dict(PIECES)
    piece = lambda key, rows=None: big_reduced[PIECE_OFFS[key]:PIECE_OFFS[key] + (rows or piece_rows[key])]
    both = lambda name, fn: jnp.stack([fn(piece(name + '0')), fn(piece(name + '1'))])
    ident, transpose = (lambda a: a), (lambda a: a.T)
    grads = {'w_in_ab': piece('w_in_t', W_IN_PIECE).T[None], 'w_glu_b': piece('w_glu').reshape(inp['w_glu_b'].shape),
             'w_out_ab': piece('w_out')[None], 'pool_w': piece('pool_w').reshape(inp['pool_w'].shape),
             'xa_wq': both('wq', ident), 'xa_wkv': both('wkv_t', transpose), 'xa_wo': both('wo', ident),
             'ffn_w_up': both('up_t', transpose), 'ffn_w_down': both('down', ident)}
    for n, off, size in zip(rep_names + small_names, misc_offs, misc_sizes):
        flat = misc_sum[off:off + _round_up(size, PACK_COLS) // PACK_COLS].reshape(-1)[:size]
        if n in REPLICATED:
            grads[n] = flat.reshape(inp[n].shape)
        else:
            shape, axis = SMALL_SHARDED[n]
            grads[n] = lax.dynamic_index_in_dim(_split_shards(flat.reshape(shape), axis), dev, 0, keepdims=False
                                                ).reshape(inp[n].shape)
    tiny_names = [n for n in WEIGHT_NAMES if n not in BIG_SHARDED]
    upd = {}
    for n in big_names:
        upd[n] = _adamw(inp[n], grads[n], inp['m_' + n], inp['v_' + n], "adamw_" + n)
    tiny_sizes = [int(np.prod(inp[n].shape)) for n in tiny_names]
    tiny_offs = _row_offsets(tiny_sizes)
    packs = [_pack_rows([src(n).astype(F32).reshape(-1) for n in tiny_names], 8)
             for src in (lambda n: inp[n], lambda n: grads[n], lambda n: inp['m_' + n], lambda n: inp['v_' + n])]
    tiny_out = _adamw(*packs, "adamw_small")
    for n, off, size in zip(tiny_names, tiny_offs, tiny_sizes):
        r = _round_up(size, PACK_COLS) // PACK_COLS
        upd[n] = tuple(o[off:off + r].reshape(-1)[:size].reshape(inp[n].shape) for o in tiny_out)

    loss = lax.psum(loss_part[0, 0], ("x", "y", "c"))
    outs = [loss, grad_x[None]]
    outs += [grads[n] for n in WEIGHT_NAMES]
    for i in range(3):
        outs += [upd[n][i] for n in WEIGHT_NAMES]
    return tuple(outs)


def _step(inp):
    full = _gather_weights(inp)
    loss_part, grad_x, big_grads, rep_grads, small_grads = _local_step(inp, full)
    big_reduced, misc_sum, misc_sizes = _reduce_gradients(big_grads, rep_grads, small_grads)
    return _update(inp, loss_part, grad_x, big_reduced, misc_sum, misc_sizes)


INPUT_NAMES = (['x', 'mem'] + WEIGHT_NAMES + ['loss_target'] + ['m_' + n for n in WEIGHT_NAMES]
               + ['v_' + n for n in WEIGHT_NAMES])


def kernel(x, mem, norm_mix_g, norm_xa_g, norm_ffn_g, norm_mem_g, norm_final_g, w_in_ab, conv_qkv_a, a_log_a, dt_bias_a, onorm_g_a, ssm_lambda_re, ssm_lambda_im, ssm_b_re, ssm_b_im, ssm_c_re, ssm_c_im, ssm_d, ssm_log_dt, w_glu_b, b_glu_b, w_out_ab, pool_w, pool_scale, xa_wq, xa_wkv, xa_wo, ffn_w_up, ffn_conv, ffn_w_down, loss_target, m_norm_mix_g, m_norm_xa_g, m_norm_ffn_g, m_norm_mem_g, m_norm_final_g, m_w_in_ab, m_conv_qkv_a, m_a_log_a, m_dt_bias_a, m_onorm_g_a, m_ssm_lambda_re, m_ssm_lambda_im, m_ssm_b_re, m_ssm_b_im, m_ssm_c_re, m_ssm_c_im, m_ssm_d, m_ssm_log_dt, m_w_glu_b, m_b_glu_b, m_w_out_ab, m_pool_w, m_pool_scale, m_xa_wq, m_xa_wkv, m_xa_wo, m_ffn_w_up, m_ffn_conv, m_ffn_w_down, v_norm_mix_g, v_norm_xa_g, v_norm_ffn_g, v_norm_mem_g, v_norm_final_g, v_w_in_ab, v_conv_qkv_a, v_a_log_a, v_dt_bias_a, v_onorm_g_a, v_ssm_lambda_re, v_ssm_lambda_im, v_ssm_b_re, v_ssm_b_im, v_ssm_c_re, v_ssm_c_im, v_ssm_d, v_ssm_log_dt, v_w_glu_b, v_b_glu_b, v_w_out_ab, v_pool_w, v_pool_scale, v_xa_wq, v_xa_wkv, v_xa_wo, v_ffn_w_up, v_ffn_conv, v_ffn_w_down):
    args = (x, mem, norm_mix_g, norm_xa_g, norm_ffn_g, norm_mem_g, norm_final_g, w_in_ab, conv_qkv_a, a_log_a, dt_bias_a, onorm_g_a, ssm_lambda_re, ssm_lambda_im, ssm_b_re, ssm_b_im, ssm_c_re, ssm_c_im, ssm_d, ssm_log_dt, w_glu_b, b_glu_b, w_out_ab, pool_w, pool_scale, xa_wq, xa_wkv, xa_wo, ffn_w_up, ffn_conv, ffn_w_down, loss_target, m_norm_mix_g, m_norm_xa_g, m_norm_ffn_g, m_norm_mem_g, m_norm_final_g, m_w_in_ab, m_conv_qkv_a, m_a_log_a, m_dt_bias_a, m_onorm_g_a, m_ssm_lambda_re, m_ssm_lambda_im, m_ssm_b_re, m_ssm_b_im, m_ssm_c_re, m_ssm_c_im, m_ssm_d, m_ssm_log_dt, m_w_glu_b, m_b_glu_b, m_w_out_ab, m_pool_w, m_pool_scale, m_xa_wq, m_xa_wkv, m_xa_wo, m_ffn_w_up, m_ffn_conv, m_ffn_w_down, v_norm_mix_g, v_norm_xa_g, v_norm_ffn_g, v_norm_mem_g, v_norm_final_g, v_w_in_ab, v_conv_qkv_a, v_a_log_a, v_dt_bias_a, v_onorm_g_a, v_ssm_lambda_re, v_ssm_lambda_im, v_ssm_b_re, v_ssm_b_im, v_ssm_c_re, v_ssm_c_im, v_ssm_d, v_ssm_log_dt, v_w_glu_b, v_b_glu_b, v_w_out_ab, v_pool_w, v_pool_scale, v_xa_wq, v_xa_wkv, v_xa_wo, v_ffn_w_up, v_ffn_conv, v_ffn_w_down)
    return _step(dict(zip(INPUT_NAMES, args)))
```

```python
import functools
import math

import numpy as np
import jax
import jax.numpy as jnp
from jax import lax
from jax.experimental import pallas as pl
from jax.experimental.pallas import tpu as pltpu

F32, BF16 = jnp.float32, jnp.bfloat16
HIGH, HIGHEST = lax.Precision.HIGH, lax.Precision.HIGHEST
MESH = pl.DeviceIdType.MESH

N_DEV = 8
SEQ, D_MODEL, MEM_LEN = 2048, 1024, 256
WIDTH_A, N_HEADS_A, HEAD_A, CONV_A = 512, 4, 128, 4
GDR_CHUNK = 128
SSM_WIDTH, SSM_GROUP, N_GROUPS, SSM_STATE = 512, 16, 32, 64
SSM_CH = N_GROUPS * SSM_STATE
SCAN_CB = 512
POOL_WINDOWS = (2, 4, 8, 16)
POOL_GROUP = 256
N_HEADS_X, HEAD_X = 4, 256
D_FF, CONV_FFN = 2816, 3
RMS_EPS = 1e-6
ADAM_LR, ADAM_B1, ADAM_B2, ADAM_EPS, ADAM_WD, ADAM_STEP = 0.001, 0.9, 0.999, 1e-08, 0.01, 10
LANE = 128
PACK_COLS = 1024
VMEM_LIMIT_BYTES = 56 * 1024 * 1024


def _params(sem=None):
    return pltpu.CompilerParams(dimension_semantics=sem, vmem_limit_bytes=VMEM_LIMIT_BYTES)


def _tile(dim, pref):
    best = None
    for t in range(LANE, min(dim, pref) + 1, LANE):
        if dim % t == 0:
            best = t
    return best if best is not None else dim


def _mm(a, b, mode, name, out_dtype=F32, res=None, tm=512, tn=512, tk=1024):
    if mode == "nn":
        (m, k), n = a.shape, b.shape[1]
    elif mode == "nt":
        (m, k), n = a.shape, b.shape[0]
    else:
        (k, m), n = a.shape, b.shape[1]
    tm, tn, tk = _tile(m, tm), _tile(n, tn), _tile(k, tk)
    nk = k // tk
    dims = {"nn": ((1,), (0,)), "nt": ((1,), (1,)), "tn": ((0,), (0,))}[mode]

    def body(*refs):
        if res is None:
            a_ref, b_ref, o_ref, acc = refs
            r_ref = None
        else:
            a_ref, b_ref, r_ref, o_ref, acc = refs
        kk = pl.program_id(2)

        @pl.when(kk == 0)
        def _():
            acc[...] = jnp.zeros_like(acc)

        acc[...] += lax.dot_general(a_ref[...].astype(BF16), b_ref[...].astype(BF16), (dims, ((), ())),
                                    preferred_element_type=F32)

        @pl.when(kk == nk - 1)
        def _():
            out = acc[...]
            if r_ref is not None:
                out = out + r_ref[...].astype(F32)
            o_ref[...] = out.astype(out_dtype)

    a_spec = (pl.BlockSpec((tk, tm), lambda i, j, q: (q, i)) if mode == "tn"
              else pl.BlockSpec((tm, tk), lambda i, j, q: (i, q)))
    b_spec = (pl.BlockSpec((tn, tk), lambda i, j, q: (j, q)) if mode == "nt"
              else pl.BlockSpec((tk, tn), lambda i, j, q: (q, j)))
    o_spec = pl.BlockSpec((tm, tn), lambda i, j, q: (i, j))
    in_specs, args = [a_spec, b_spec], [a, b]
    if res is not None:
        in_specs.append(o_spec)
        args.append(res)
    return pl.pallas_call(
        body, name=name, grid=(m // tm, n // tn, nk), in_specs=in_specs, out_specs=o_spec,
        out_shape=jax.ShapeDtypeStruct((m, n), out_dtype), scratch_shapes=[pltpu.VMEM((tm, tn), F32)],
        compiler_params=_params(("parallel", "parallel", "arbitrary")))(*args)


def _rms_fwd(x, g, out_dtype, name, tr=256):
    rows, d = x.shape

    def body(x_ref, g_ref, o_ref):
        xv = x_ref[...]
        r = lax.rsqrt(jnp.mean(xv * xv, axis=-1, keepdims=True) + RMS_EPS)
        o_ref[...] = (xv * r * g_ref[...]).astype(out_dtype)

    return pl.pallas_call(
        body, name=name, grid=(rows // tr,),
        in_specs=[pl.BlockSpec((tr, d), lambda i: (i, 0)), pl.BlockSpec((1, d), lambda i: (0, 0))],
        out_specs=pl.BlockSpec((tr, d), lambda i: (i, 0)), out_shape=jax.ShapeDtypeStruct((rows, d), out_dtype),
        compiler_params=_params(("parallel",)))(x, g.reshape(1, d))


def _rms_bwd(x, g, dy, dres, name, tr=256):
    rows, d = x.shape

    def body(*refs):
        if dres is None:
            x_ref, g_ref, dy_ref, dx_ref, dg_ref = refs
            r_ref = None
        else:
            x_ref, g_ref, dy_ref, r_ref, dx_ref, dg_ref = refs

        @pl.when(pl.program_id(0) == 0)
        def _():
            dg_ref[...] = jnp.zeros_like(dg_ref)

        xv, dyv = x_ref[...], dy_ref[...].astype(F32)
        r = lax.rsqrt(jnp.mean(xv * xv, axis=-1, keepdims=True) + RMS_EPS)
        xh = xv * r
        dyg = dyv * g_ref[...]
        dx = r * (dyg - xh * jnp.mean(dyg * xh, axis=-1, keepdims=True))
        if r_ref is not None:
            dx = dx + r_ref[...]
        dx_ref[...] = dx
        dg_ref[...] += jnp.sum(dyv * xh, axis=0, keepdims=True)

    blk = pl.BlockSpec((tr, d), lambda i: (i, 0))
    vec = pl.BlockSpec((1, d), lambda i: (0, 0))
    in_specs, args = [blk, vec, blk], [x, g.reshape(1, d), dy]
    if dres is not None:
        in_specs.append(blk)
        args.append(dres)
    return pl.pallas_call(
        body, name=name, grid=(rows // tr,), in_specs=in_specs, out_specs=[blk, vec],
        out_shape=[jax.ShapeDtypeStruct((rows, d), F32), jax.ShapeDtypeStruct((1, d), F32)],
        compiler_params=_params(("arbitrary",)))(*args)


def _loss_head(x, g, target, name, tr=256):
    rows, d = x.shape

    def body(x_ref, g_ref, t_ref, loss_ref, dx_ref, dg_ref):
        @pl.when(pl.program_id(0) == 0)
        def _():
            dg_ref[...] = jnp.zeros_like(dg_ref)
            loss_ref[...] = jnp.zeros_like(loss_ref)

        xv = x_ref[...]
        r = lax.rsqrt(jnp.mean(xv * xv, axis=-1, keepdims=True) + RMS_EPS)
        xh = xv * r
        err = xh * g_ref[...] - t_ref[...]
        loss_ref[...] += 0.5 * jnp.sum(jnp.mean(err * err, axis=-1, keepdims=True), keepdims=True)
        dyv = err * (1.0 / d)
        dyg = dyv * g_ref[...]
        dx_ref[...] = r * (dyg - xh * jnp.mean(dyg * xh, axis=-1, keepdims=True))
        dg_ref[...] += jnp.sum(dyv * xh, axis=0, keepdims=True)

    blk = pl.BlockSpec((tr, d), lambda i: (i, 0))
    vec = pl.BlockSpec((1, d), lambda i: (0, 0))
    return pl.pallas_call(
        body, name=name, grid=(rows // tr,), in_specs=[blk, vec, blk],
        out_specs=[pl.BlockSpec((1, 1), lambda i: (0, 0)), blk, vec],
        out_shape=[jax.ShapeDtypeStruct((1, 1), F32), jax.ShapeDtypeStruct((rows, d), F32),
                   jax.ShapeDtypeStruct((1, d), F32)],
        compiler_params=_params(("arbitrary",)))(x, g.reshape(1, d), target)


def _shift_down(x, s):
    rows = lax.broadcasted_iota(jnp.int32, x.shape, 0)
    return jnp.where(rows >= s, pltpu.roll(x, s, 0), 0.0)


def _shift_up(x, s):
    n = x.shape[0]
    rows = lax.broadcasted_iota(jnp.int32, x.shape, 0)
    return jnp.where(rows < n - s, pltpu.roll(x, n - s, 0), 0.0)


def _sigmoid(x):
    return 1.0 / (1.0 + jnp.exp(-x))


def _silu_and_grad(x):
    s = _sigmoid(x)
    return x * s, s * (1.0 + x * (1.0 - s))


_GELU_C0, _GELU_C1 = math.sqrt(2.0 / math.pi), 0.044715


def _gelu_and_grad(x):
    th = jnp.tanh(_GELU_C0 * (x + _GELU_C1 * x * x * x))
    y = 0.5 * x * (1.0 + th)
    dy = 0.5 * (1.0 + th) + 0.5 * x * (1.0 - th * th) * _GELU_C0 * (1.0 + 3.0 * _GELU_C1 * x * x)
    return y, dy


def _ffn_act_fwd(h, w, name, tc=256):
    t = h.shape[0]
    nb = D_FF // tc

    def body(hg_ref, hv_ref, wg_ref, wv_ref, a_ref):
        def conv(x, wr):
            return wr[2:3, :] * x + wr[1:2, :] * _shift_down(x, 1) + wr[0:1, :] * _shift_down(x, 2)

        cg = conv(hg_ref[...], wg_ref[...])
        cv = conv(hv_ref[...], wv_ref[...])
        a_ref[...] = (cg * _sigmoid(cg) * cv).astype(BF16)

    return pl.pallas_call(
        body, name=name, grid=(nb,),
        in_specs=[pl.BlockSpec((t, tc), lambda j: (0, j)), pl.BlockSpec((t, tc), lambda j: (0, j + nb)),
                  pl.BlockSpec((CONV_FFN, tc), lambda j: (0, j)), pl.BlockSpec((CONV_FFN, tc), lambda j: (0, j + nb))],
        out_specs=pl.BlockSpec((t, tc), lambda j: (0, j)), out_shape=jax.ShapeDtypeStruct((t, D_FF), BF16),
        compiler_params=_params(("parallel",)))(h, h, w, w)


def _ffn_act_bwd(h, w, da, name, tc=256):
    t = h.shape[0]
    nb = D_FF // tc

    def body(hg_ref, hv_ref, wg_ref, wv_ref, da_ref, dhg_ref, dhv_ref, dwg_ref, dwv_ref):
        hg, hv, wg, wv = hg_ref[...], hv_ref[...], wg_ref[...], wv_ref[...]
        hg1, hg2, hv1, hv2 = _shift_down(hg, 1), _shift_down(hg, 2), _shift_down(hv, 1), _shift_down(hv, 2)
        cg = wg[2:3, :] * hg + wg[1:2, :] * hg1 + wg[0:1, :] * hg2
        cv = wv[2:3, :] * hv + wv[1:2, :] * hv1 + wv[0:1, :] * hv2
        sg, dsg = _silu_and_grad(cg)
        dav = da_ref[...].astype(F32)
        dcv = dav * sg
        dcg = dav * cv * dsg

        def conv_t(dc, wr):
            return wr[2:3, :] * dc + wr[1:2, :] * _shift_up(dc, 1) + wr[0:1, :] * _shift_up(dc, 2)

        dhg_ref[...] = conv_t(dcg, wg).astype(BF16)
        dhv_ref[...] = conv_t(dcv, wv).astype(BF16)
        dwg_ref[0:1, :] = jnp.sum(dcg * hg2, axis=0, keepdims=True)
        dwg_ref[1:2, :] = jnp.sum(dcg * hg1, axis=0, keepdims=True)
        dwg_ref[2:3, :] = jnp.sum(dcg * hg, axis=0, keepdims=True)
        dwv_ref[0:1, :] = jnp.sum(dcv * hv2, axis=0, keepdims=True)
        dwv_ref[1:2, :] = jnp.sum(dcv * hv1, axis=0, keepdims=True)
        dwv_ref[2:3, :] = jnp.sum(dcv * hv, axis=0, keepdims=True)

    big = lambda off: pl.BlockSpec((t, tc), lambda j: (0, j + off))
    small = lambda off: pl.BlockSpec((CONV_FFN, tc), lambda j: (0, j + off))
    dhg, dhv, dwg, dwv = pl.pallas_call(
        body, name=name, grid=(nb,),
        in_specs=[big(0), big(nb), small(0), small(nb), big(0)],
        out_specs=[big(0), big(0), small(0), small(0)],
        out_shape=[jax.ShapeDtypeStruct((t, D_FF), BF16), jax.ShapeDtypeStruct((t, D_FF), BF16),
                   jax.ShapeDtypeStruct((CONV_FFN, D_FF), F32), jax.ShapeDtypeStruct((CONV_FFN, D_FF), F32)],
        compiler_params=_params(("parallel",)))(h, h, w, w, da)
    return jnp.concatenate([dhg, dhv], axis=1), jnp.concatenate([dwg, dwv], axis=1)


def _attn_probs(q, k):
    s = lax.dot_general(q.astype(BF16), k.astype(BF16), (((1,), (1,)), ((), ())),
                        preferred_element_type=F32) * (HEAD_X ** -0.5)
    s = s - jnp.max(s, axis=-1, keepdims=True)
    p = jnp.exp(s)
    return p / jnp.sum(p, axis=-1, keepdims=True)


def _attn_fwd(q, kv, name, tq=512):
    t = q.shape[0]

    def body(q_ref, k_ref, v_ref, o_ref):
        p = _attn_probs(q_ref[...], k_ref[...])
        o_ref[...] = jnp.dot(p.astype(BF16), v_ref[...].astype(BF16), preferred_element_type=F32).astype(BF16)

    return pl.pallas_call(
        body, name=name, grid=(N_HEADS_X, t // tq),
        in_specs=[pl.BlockSpec((tq, HEAD_X), lambda h, i: (i, h)),
                  pl.BlockSpec((MEM_LEN, HEAD_X), lambda h, i: (0, h)),
                  pl.BlockSpec((MEM_LEN, HEAD_X), lambda h, i: (0, h + N_HEADS_X))],
        out_specs=pl.BlockSpec((tq, HEAD_X), lambda h, i: (i, h)),
        out_shape=jax.ShapeDtypeStruct((t, N_HEADS_X * HEAD_X), BF16),
        compiler_params=_params(("parallel", "parallel")))(q, kv, kv)


def _attn_bwd(q, kv, do, name, tq=512):
    t = q.shape[0]

    def body(q_ref, k_ref, v_ref, do_ref, dq_ref, dk_ref, dv_ref):
        @pl.when(pl.program_id(1) == 0)
        def _():
            dk_ref[...] = jnp.zeros_like(dk_ref)
            dv_ref[...] = jnp.zeros_like(dv_ref)

        qb, kb, vb, dob = (r[...].astype(BF16) for r in (q_ref, k_ref, v_ref, do_ref))
        p = _attn_probs(qb, kb)
        dp = lax.dot_general(dob, vb, (((1,), (1,)), ((), ())), preferred_element_type=F32)
        ds = p * (dp - jnp.sum(dp * p, axis=-1, keepdims=True)) * (HEAD_X ** -0.5)
        dsb = ds.astype(BF16)
        dq_ref[...] = jnp.dot(dsb, kb, preferred_element_type=F32).astype(BF16)
        dk_ref[...] += lax.dot_general(dsb, qb, (((0,), (0,)), ((), ())), preferred_element_type=F32)
        dv_ref[...] += lax.dot_general(p.astype(BF16), dob, (((0,), (0,)), ((), ())), preferred_element_type=F32)

    qs = pl.BlockSpec((tq, HEAD_X), lambda h, i: (i, h))
    ms = pl.BlockSpec((MEM_LEN, HEAD_X), lambda h, i: (0, h))
    return pl.pallas_call(
        body, name=name, grid=(N_HEADS_X, t // tq),
        in_specs=[qs, ms, pl.BlockSpec((MEM_LEN, HEAD_X), lambda h, i: (0, h + N_HEADS_X)), qs],
        out_specs=[qs, ms, ms],
        out_shape=[jax.ShapeDtypeStruct((t, D_MODEL), BF16), jax.ShapeDtypeStruct((MEM_LEN, D_MODEL), F32),
                   jax.ShapeDtypeStruct((MEM_LEN, D_MODEL), F32)],
        compiler_params=_params(("parallel", "arbitrary")))(q, kv, kv, do)


def _pool_counts(t, win):
    pos = lax.broadcasted_iota(jnp.int32, (t, 1), 0).astype(F32) + 1.0
    return 1.0 / jnp.minimum(pos, float(win))


def _pool_delta(xv, win):
    s, step = xv, 1
    while step < win:
        s = s + _shift_down(s, step)
        step *= 2
    return s * _pool_counts(xv.shape[0], win) - xv


def _pool_delta_t(dv, win):
    s, step = dv * _pool_counts(dv.shape[0], win), 1
    while step < win:
        s = s + _shift_up(s, step)
        step *= 2
    return s - dv


def _pool_fwd(xn, w, scale, res, name):
    t = xn.shape[0]

    def make_branch(win, xn_ref, w_ref, s_ref, r_ref, o_ref):
        def branch():
            dl = _pool_delta(xn_ref[...], win)
            y = jnp.dot(dl.astype(BF16), w_ref[0], preferred_element_type=F32)
            o_ref[...] = r_ref[...] + y * s_ref[...]
        return branch

    def body(xn_ref, w_ref, s_ref, r_ref, o_ref):
        for gi, win in enumerate(POOL_WINDOWS):
            pl.when(pl.program_id(0) == gi)(make_branch(win, xn_ref, w_ref, s_ref, r_ref, o_ref))

    blk = pl.BlockSpec((t, POOL_GROUP), lambda g: (0, g))
    return pl.pallas_call(
        body, name=name, grid=(len(POOL_WINDOWS),),
        in_specs=[blk, pl.BlockSpec((1, POOL_GROUP, POOL_GROUP), lambda g: (g, 0, 0)),
                  pl.BlockSpec((1, POOL_GROUP), lambda g: (0, g)), blk],
        out_specs=blk, out_shape=jax.ShapeDtypeStruct((t, D_MODEL), F32),
        compiler_params=_params(("parallel",)))(xn, w, scale, res)


def _pool_bwd(xn, w, scale, dmix, name):
    t = xn.shape[0]

    def make_branch(win, xn_ref, w_ref, s_ref, d_ref, dxn_ref, dw_ref, ds_ref):
        def branch():
            dl = _pool_delta(xn_ref[...], win).astype(BF16)
            wv = w_ref[0]
            dm = d_ref[...]
            y = jnp.dot(dl, wv, preferred_element_type=F32)
            ds_ref[...] = jnp.sum(dm * y, axis=0, keepdims=True)
            dy = (dm * s_ref[...]).astype(BF16)
            dw_ref[0] = lax.dot_general(dl, dy, (((0,), (0,)), ((), ())), preferred_element_type=F32)
            ddl = lax.dot_general(dy, wv, (((1,), (1,)), ((), ())), preferred_element_type=F32)
            dxn_ref[...] = _pool_delta_t(ddl, win)
        return branch

    def body(*refs):
        for gi, win in enumerate(POOL_WINDOWS):
            pl.when(pl.program_id(0) == gi)(make_branch(win, *refs))

    blk = pl.BlockSpec((t, POOL_GROUP), lambda g: (0, g))
    wspec = pl.BlockSpec((1, POOL_GROUP, POOL_GROUP), lambda g: (g, 0, 0))
    vec = pl.BlockSpec((1, POOL_GROUP), lambda g: (0, g))
    return pl.pallas_call(
        body, name=name, grid=(len(POOL_WINDOWS),), in_specs=[blk, wspec, vec, blk], out_specs=[blk, wspec, vec],
        out_shape=[jax.ShapeDtypeStruct((t, D_MODEL), F32),
                   jax.ShapeDtypeStruct((len(POOL_WINDOWS), POOL_GROUP, POOL_GROUP), F32),
                   jax.ShapeDtypeStruct((1, D_MODEL), F32)],
        compiler_params=_params(("parallel",)))(xn, w, scale, dmix)


def _qkv_conv(h, wr):
    return (wr[3:4, :] * h + wr[2:3, :] * _shift_down(h, 1) + wr[1:2, :] * _shift_down(h, 2)
            + wr[0:1, :] * _shift_down(h, 3))


def _qkv_pre_fwd(h, w, col0, ncols, normalize, scale, name):
    t = h.shape[0]

    def body(h_ref, w_ref, o_ref):
        c = _qkv_conv(h_ref[...], w_ref[...])
        s = c * _sigmoid(c)
        if normalize:
            s = s * lax.rsqrt(jnp.sum(s * s, axis=-1, keepdims=True) + 1e-6) * scale
        o_ref[...] = s

    return pl.pallas_call(
        body, name=name, grid=(ncols,),
        in_specs=[pl.BlockSpec((t, HEAD_A), lambda j: (0, j + col0)), pl.BlockSpec((CONV_A, HEAD_A), lambda j: (0, j + col0))],
        out_specs=pl.BlockSpec((t, HEAD_A), lambda j: (0, j)), out_shape=jax.ShapeDtypeStruct((t, ncols * HEAD_A), F32),
        compiler_params=_params(("parallel",)))(h, w)


def _qkv_pre_bwd(h, w, dy, col0, ncols, normalize, scale, name):
    t = h.shape[0]

    def body(h_ref, w_ref, dy_ref, dh_ref, dw_ref):
        hv, wr, dyv = h_ref[...], w_ref[...], dy_ref[...]
        h1, h2, h3 = _shift_down(hv, 1), _shift_down(hv, 2), _shift_down(hv, 3)
        c = wr[3:4, :] * hv + wr[2:3, :] * h1 + wr[1:2, :] * h2 + wr[0:1, :] * h3
        s, dsilu = _silu_and_grad(c)
        if normalize:
            r = lax.rsqrt(jnp.sum(s * s, axis=-1, keepdims=True) + 1e-6)
            y = s * r
            dyv = dyv * scale
            ds = r * (dyv - y * jnp.sum(dyv * y, axis=-1, keepdims=True))
        else:
            ds = dyv
        dc = ds * dsilu
        dh = (wr[3:4, :] * dc + wr[2:3, :] * _shift_up(dc, 1) + wr[1:2, :] * _shift_up(dc, 2)
              + wr[0:1, :] * _shift_up(dc, 3))
        dh_ref[...] = dh.astype(BF16)
        dw_ref[0:1, :] = jnp.sum(dc * h3, axis=0, keepdims=True)
        dw_ref[1:2, :] = jnp.sum(dc * h2, axis=0, keepdims=True)
        dw_ref[2:3, :] = jnp.sum(dc * h1, axis=0, keepdims=True)
        dw_ref[3:4, :] = jnp.sum(dc * hv, axis=0, keepdims=True)

    return pl.pallas_call(
        body, name=name, grid=(ncols,),
        in_specs=[pl.BlockSpec((t, HEAD_A), lambda j: (0, j + col0)), pl.BlockSpec((CONV_A, HEAD_A), lambda j: (0, j + col0)),
                  pl.BlockSpec((t, HEAD_A), lambda j: (0, j))],
        out_specs=[pl.BlockSpec((t, HEAD_A), lambda j: (0, j)), pl.BlockSpec((CONV_A, HEAD_A), lambda j: (0, j))],
        out_shape=[jax.ShapeDtypeStruct((t, ncols * HEAD_A), BF16), jax.ShapeDtypeStruct((CONV_A, ncols * HEAD_A), F32)],
        compiler_params=_params(("parallel",)))(h, w, dy)


def _softplus(x):
    return jnp.maximum(x, 0.0) + jnp.log1p(jnp.exp(-jnp.abs(x)))


def _gates_fwd(ba, arow, brow, name):
    t = ba.shape[0]

    def body(x_ref, a_ref, b_ref, o_ref):
        xv = x_ref[...]
        lane = lax.broadcasted_iota(jnp.int32, xv.shape, 1)
        beta = _sigmoid(xv)
        g = -jnp.exp(a_ref[...]) * _softplus(xv + b_ref[...])
        o_ref[...] = jnp.where(lane < N_HEADS_A, beta, jnp.where(lane < 2 * N_HEADS_A, g, 0.0))

    return pl.pallas_call(body, name=name, out_shape=jax.ShapeDtypeStruct((t, LANE), F32),
                          compiler_params=_params())(ba, arow, brow)


def _gates_bwd(ba, arow, brow, dgb, name):
    t = ba.shape[0]

    def body(x_ref, a_ref, b_ref, d_ref, dx_ref, da_ref, db_ref):
        xv, dv = x_ref[...], d_ref[...]
        lane = lax.broadcasted_iota(jnp.int32, xv.shape, 1)
        beta = _sigmoid(xv)
        ea = jnp.exp(a_ref[...])
        z = xv + b_ref[...]
        dgv = jnp.where((lane >= N_HEADS_A) & (lane < 2 * N_HEADS_A), dv, 0.0) * (-ea)
        dz = dgv * _sigmoid(z)
        dx = jnp.where(lane < N_HEADS_A, dv * beta * (1.0 - beta), dz)
        dx_ref[...] = dx.astype(BF16)
        db_ref[...] = jnp.sum(dz, axis=0, keepdims=True)
        da_ref[...] = jnp.sum(dgv * _softplus(z), axis=0, keepdims=True)

    return pl.pallas_call(
        body, name=name,
        out_shape=[jax.ShapeDtypeStruct((t, LANE), BF16), jax.ShapeDtypeStruct((1, LANE), F32),
                   jax.ShapeDtypeStruct((1, LANE), F32)],
        compiler_params=_params())(ba, arow, brow, dgb)


def _dot(a, b, prec=None):
    if prec is None:
        return jnp.dot(a.astype(BF16), b.astype(BF16), preferred_element_type=F32)
    return jnp.dot(a, b, precision=prec, preferred_element_type=F32)


def _dot_nt(a, b, prec=None):
    if prec is None:
        a, b = a.astype(BF16), b.astype(BF16)
    return lax.dot_general(a, b, (((1,), (1,)), ((), ())), precision=prec, preferred_element_type=F32)


def _dot_tn(a, b, prec=None):
    if prec is None:
        a, b = a.astype(BF16), b.astype(BF16)
    return lax.dot_general(a, b, (((0,), (0,)), ((), ())), precision=prec, preferred_element_type=F32)


def _gdr_chunk_terms(k, beta, g):
    c = GDR_CHUNK
    row = lax.broadcasted_iota(jnp.int32, (c, c), 0)
    col = lax.broadcasted_iota(jnp.int32, (c, c), 1)
    causal, strict = row >= col, row > col
    gcum = _dot(causal.astype(F32), jnp.broadcast_to(g, (c, c)), HIGHEST)
    diff = gcum - gcum.T
    decay = jnp.where(causal, jnp.exp(jnp.where(causal, diff, 0.0)), 0.0)
    kb = k * beta
    kk = _dot_nt(kb, k)
    return row, col, causal, strict, gcum, decay, kb, kk


def _unit_lower_inverse(a):
    c = a.shape[0]
    eye = (lax.broadcasted_iota(jnp.int32, (c, c), 0) == lax.broadcasted_iota(jnp.int32, (c, c), 1)).astype(F32)
    p = -a
    inv = eye + p
    step = 1
    while 2 * step < c:
        p = _dot(p, p, HIGH)
        inv = inv + _dot(inv, p, HIGH)
        step *= 2
    return inv


def _gdr_fwd(q, k, v, gb, name):
    t = q.shape[0]
    c = GDR_CHUNK
    n = t // c

    def body(q_ref, k_ref, v_ref, gb_ref, o_ref, tm_ref, s_ref, state):
        @pl.when(pl.program_id(1) == 0)
        def _():
            state[...] = jnp.zeros_like(state)

        qv, kv, vv = q_ref[...], k_ref[...], v_ref[...]
        beta, g = gb_ref[0, :, 0:1], gb_ref[0, :, 1:2]
        row, col, causal, strict, gcum, decay, kb, kk = _gdr_chunk_terms(kv, beta, g)
        tm = _unit_lower_inverse(jnp.where(strict, kk * decay, 0.0))
        e = jnp.exp(gcum)
        u = _dot(tm, vv * beta, HIGH)
        w = _dot(tm, kb * e, HIGH)
        p = jnp.where(causal, _dot_nt(qv, kv) * decay, 0.0)
        s = state[...]
        s_ref[0, 0] = s
        tm_ref[0, 0] = tm
        vn = u - _dot(w, s)
        o_ref[...] = _dot(qv * e, s) + _dot(p, vn)
        glast = gcum[c - 1:c, :]
        state[...] = s * jnp.exp(glast) + _dot_tn(kv * jnp.exp(glast - gcum), vn)

    blk = pl.BlockSpec((c, HEAD_A), lambda h, i: (i, h))
    mat = pl.BlockSpec((1, 1, c, c), lambda h, i: (h, i, 0, 0))
    return pl.pallas_call(
        body, name=name, grid=(N_HEADS_A, n),
        in_specs=[blk, blk, blk, pl.BlockSpec((1, c, 2), lambda h, i: (h, i, 0))],
        out_specs=[blk, mat, mat],
        out_shape=[jax.ShapeDtypeStruct((t, WIDTH_A), F32), jax.ShapeDtypeStruct((N_HEADS_A, n, c, c), F32),
                   jax.ShapeDtypeStruct((N_HEADS_A, n, HEAD_A, HEAD_A), F32)],
        scratch_shapes=[pltpu.VMEM((HEAD_A, HEAD_A), F32)],
        compiler_params=_params(("parallel", "arbitrary")))(q, k, v, gb)


def _gdr_bwd(q, k, v, gb, tm_all, s_all, do, name):
    t = q.shape[0]
    c = GDR_CHUNK
    n = t // c

    def body(q_ref, k_ref, v_ref, gb_ref, tm_ref, s_ref, do_ref, dq_ref, dk_ref, dv_ref, dgb_ref, dstate):
        @pl.when(pl.program_id(1) == 0)
        def _():
            dstate[...] = jnp.zeros_like(dstate)

        qv, kv, vv, dov = q_ref[...], k_ref[...], v_ref[...], do_ref[...]
        beta, g = gb_ref[0, :, 0:1], gb_ref[0, :, 1:2]
        tm, s, dsp = tm_ref[0, 0], s_ref[0, 0], dstate[...]
        row, col, causal, strict, gcum, decay, kb, kk = _gdr_chunk_terms(kv, beta, g)
        e = jnp.exp(gcum)
        vb, kbe = vv * beta, kb * e
        u = _dot(tm, vb, HIGH)
        w = _dot(tm, kbe, HIGH)
        qk = _dot_nt(qv, kv)
        p = jnp.where(causal, qk * decay, 0.0)
        vn = u - _dot(w, s)
        glast = gcum[c - 1:c, :]
        el = jnp.exp(glast)
        f = jnp.exp(glast - gcum)
        kd = kv * f
        qe = qv * e

        dvn = _dot_tn(p, dov) + _dot(kd, dsp)
        dglast = el[:, 0:1] * jnp.sum(s * dsp, keepdims=True)
        dkd = _dot_nt(vn, dsp)
        dk = dkd * f
        df = jnp.sum(dkd * kv, axis=1, keepdims=True) * f[:, 0:1]
        dglast = dglast + jnp.sum(df, keepdims=True)
        dgc = -df
        dp = jnp.where(causal, _dot_nt(dov, vn), 0.0)
        dqe = _dot_nt(dov, s)
        dq = dqe * e
        de = jnp.sum(dqe * qv, axis=1, keepdims=True)
        dstate[...] = dsp * el + _dot_tn(qe, dov) - _dot_tn(w, dvn)
        dw = -_dot_nt(dvn, s)
        dvb = _dot_tn(tm, dvn, HIGH)
        dkbe = _dot_tn(tm, dw, HIGH)
        da = -jnp.where(strict, _dot_nt(dvb, u) + _dot_nt(dkbe, w), 0.0)
        dkk = da * decay
        dqk = dp * decay
        dd = da * kk + dp * qk
        dq = dq + _dot(dqk, kv)
        dk = dk + _dot_tn(dqk, qv)
        dkb = _dot(dkk, kv) + dkbe * e
        dk = dk + _dot_tn(dkk, kb)
        de = de + jnp.sum(dkbe * kb, axis=1, keepdims=True)
        dk = dk + dkb * beta
        dbeta = jnp.sum(dkb * kv, axis=1, keepdims=True) + jnp.sum(dvb * vv, axis=1, keepdims=True)
        m = dd * decay
        dgc = dgc + jnp.sum(m, axis=1, keepdims=True) - jnp.sum(m.T, axis=1, keepdims=True)
        dgc = dgc + de * e[:, 0:1]
        dgc = dgc + jnp.where(row[:, 0:1] == c - 1, dglast, 0.0)
        dg = _dot((row <= col).astype(F32), jnp.broadcast_to(dgc, (c, c)), HIGHEST)
        dq_ref[...] = dq
        dk_ref[...] = dk
        dv_ref[...] = dvb * beta
        dgb_ref[0, :, 0:1] = dbeta
        dgb_ref[0, :, 1:2] = dg[:, 0:1]

    blk = pl.BlockSpec((c, HEAD_A), lambda h, i: (n - 1 - i, h))
    mat = pl.BlockSpec((1, 1, c, c), lambda h, i: (h, n - 1 - i, 0, 0))
    gsp = pl.BlockSpec((1, c, 2), lambda h, i: (h, n - 1 - i, 0))
    return pl.pallas_call(
        body, name=name, grid=(N_HEADS_A, n),
        in_specs=[blk, blk, blk, gsp, mat, mat, blk], out_specs=[blk, blk, blk, gsp],
        out_shape=[jax.ShapeDtypeStruct((t, WIDTH_A), F32)] * 3 + [jax.ShapeDtypeStruct((N_HEADS_A, t, 2), F32)],
        scratch_shapes=[pltpu.VMEM((HEAD_A, HEAD_A), F32)],
        compiler_params=_params(("parallel", "arbitrary")))(q, k, v, gb, tm_all, s_all, do)


def _onorm_fwd(o, gate, g, name):
    t = o.shape[0]

    def body(o_ref, gate_ref, g_ref, y_ref):
        ov, gv = o_ref[...], gate_ref[...]
        r = lax.rsqrt(jnp.mean(ov * ov, axis=-1, keepdims=True) + RMS_EPS)
        y_ref[...] = (ov * r * g_ref[...] * gv * _sigmoid(gv)).astype(BF16)

    blk = pl.BlockSpec((t, HEAD_A), lambda j: (0, j))
    return pl.pallas_call(
        body, name=name, grid=(N_HEADS_A,), in_specs=[blk, blk, pl.BlockSpec((1, HEAD_A), lambda j: (0, 0))],
        out_specs=blk, out_shape=jax.ShapeDtypeStruct((t, WIDTH_A), BF16),
        compiler_params=_params(("parallel",)))(o, gate, g)


def _onorm_bwd(o, gate, g, dy, name):
    t = o.shape[0]

    def body(o_ref, gate_ref, g_ref, dy_ref, do_ref, dgate_ref, dg_ref):
        @pl.when(pl.program_id(0) == 0)
        def _():
            dg_ref[...] = jnp.zeros_like(dg_ref)

        ov, gv, dyv = o_ref[...], gate_ref[...], dy_ref[...].astype(F32)
        r = lax.rsqrt(jnp.mean(ov * ov, axis=-1, keepdims=True) + RMS_EPS)
        oh = ov * r
        sg, dsg = _silu_and_grad(gv)
        dgate_ref[...] = (dyv * oh * g_ref[...] * dsg).astype(BF16)
        dn = dyv * sg
        dg_ref[...] += jnp.sum(dn * oh, axis=0, keepdims=True)
        dng = dn * g_ref[...]
        do_ref[...] = r * (dng - oh * jnp.mean(dng * oh, axis=-1, keepdims=True))

    blk = pl.BlockSpec((t, HEAD_A), lambda j: (0, j))
    vec = pl.BlockSpec((1, HEAD_A), lambda j: (0, 0))
    return pl.pallas_call(
        body, name=name, grid=(N_HEADS_A,), in_specs=[blk, blk, vec, blk], out_specs=[blk, blk, vec],
        out_shape=[jax.ShapeDtypeStruct((t, WIDTH_A), F32), jax.ShapeDtypeStruct((t, WIDTH_A), BF16),
                   jax.ShapeDtypeStruct((1, HEAD_A), F32)],
        compiler_params=_params(("arbitrary",)))(o, gate, g, dy)


def _cmul(ar, ai, br, bi):
    return ar * br - ai * bi, ar * bi + ai * br


def _scan_tables(ar, ai, reverse):
    p1 = (ar, ai)
    p2 = _cmul(*p1, *p1)
    p4 = _cmul(*p2, *p2)
    p8 = _cmul(*p4, *p4)
    p3 = _cmul(*p2, *p1)
    p5 = _cmul(*p4, *p1)
    p6 = _cmul(*p4, *p2)
    p7 = _cmul(*p4, *p3)
    pows = [p1, p2, p3, p4, p5, p6, p7, p8]
    rows = lax.broadcasted_iota(jnp.int32, (8, ar.shape[1]), 0)
    tr = jnp.zeros((8, ar.shape[1]), F32)
    ti = jnp.zeros((8, ar.shape[1]), F32)
    for r in range(8):
        pw = pows[7 - r] if reverse else pows[r]
        tr = jnp.where(rows == r, pw[0], tr)
        ti = jnp.where(rows == r, pw[1], ti)
    return p1, p2, p4, p8, tr, ti


def _tile_scan(xr, xi, p1, p2, p4, reverse):
    rows = lax.broadcasted_iota(jnp.int32, xr.shape, 0)
    for s, (pr, pi) in ((1, p1), (2, p2), (4, p4)):
        if reverse:
            keep = rows < 8 - s
            sr, si = pltpu.roll(xr, 8 - s, 0), pltpu.roll(xi, 8 - s, 0)
        else:
            keep = rows >= s
            sr, si = pltpu.roll(xr, s, 0), pltpu.roll(xi, s, 0)
        sr, si = jnp.where(keep, sr, 0.0), jnp.where(keep, si, 0.0)
        mr, mi = _cmul(pr, pi, sr, si)
        xr, xi = xr + mr, xi + mi
    return xr, xi


def _s5_scan_fwd(bu, a, name, tb=512):
    t = bu.shape[0]
    cb = SCAN_CB
    nt = t // tb

    def body(b_ref, a_ref, x_ref, carry):
        @pl.when(pl.program_id(1) == 0)
        def _():
            carry[...] = jnp.zeros_like(carry)

        ar, ai = a_ref[:, 0:cb], a_ref[:, cb:2 * cb]
        p1, p2, p4, p8, tr, ti = _scan_tables(ar, ai, False)

        def step(j, c):
            cr, ci = c
            i = pl.multiple_of(j * 8, 8)
            xr, xi = _tile_scan(b_ref[pl.ds(i, 8), 0:cb], b_ref[pl.ds(i, 8), cb:2 * cb], p1, p2, p4, False)
            mr, mi = _cmul(tr, ti, cr, ci)
            xr, xi = xr + mr, xi + mi
            x_ref[pl.ds(i, 8), 0:cb] = xr
            x_ref[pl.ds(i, 8), cb:2 * cb] = xi
            return xr[7:8, :], xi[7:8, :]

        cr, ci = lax.fori_loop(0, tb // 8, step, (carry[0:1, :], carry[1:2, :]), unroll=2)
        carry[0:1, :] = cr
        carry[1:2, :] = ci

    blk = pl.BlockSpec((tb, 2 * cb), lambda j, i: (i, j))
    return pl.pallas_call(
        body, name=name, grid=(SSM_CH // cb, nt),
        in_specs=[blk, pl.BlockSpec((1, 2 * cb), lambda j, i: (0, j))], out_specs=blk,
        out_shape=jax.ShapeDtypeStruct((t, 2 * SSM_CH), F32), scratch_shapes=[pltpu.VMEM((8, cb), F32)],
        compiler_params=_params(("parallel", "arbitrary")))(bu, a)


def _s5_scan_bwd(dx, x, a, name, tb=512):
    t = dx.shape[0]
    cb = SCAN_CB
    nt = t // tb
    nj = tb // 8

    def body(d_ref, x_ref, xp_ref, a_ref, l_ref, da_ref, carry, acc):
        tblk = pl.program_id(1)

        @pl.when(tblk == 0)
        def _():
            carry[...] = jnp.zeros_like(carry)
            acc[...] = jnp.zeros_like(acc)

        ar, ai = a_ref[:, 0:cb], a_ref[:, cb:2 * cb]
        p1, p2, p4, p8, tr, ti = _scan_tables(ar, -ai, True)
        rows = lax.broadcasted_iota(jnp.int32, (8, cb), 0)

        def step(jj, c):
            cr, ci, sr_acc, si_acc = c
            j = nj - 1 - jj
            i = pl.multiple_of(j * 8, 8)
            lr, li = _tile_scan(d_ref[pl.ds(i, 8), 0:cb], d_ref[pl.ds(i, 8), cb:2 * cb], p1, p2, p4, True)
            mr, mi = _cmul(tr, ti, cr, ci)
            lr, li = lr + mr, li + mi
            l_ref[pl.ds(i, 8), 0:cb] = lr
            l_ref[pl.ds(i, 8), cb:2 * cb] = li
            ip = pl.multiple_of(jnp.maximum(j - 1, 0) * 8, 8)
            prev_r = jnp.where(j > 0, x_ref[pl.ds(ip, 8), 0:cb], xp_ref[:, 0:cb])
            prev_i = jnp.where(j > 0, x_ref[pl.ds(ip, 8), cb:2 * cb], xp_ref[:, cb:2 * cb])
            edge = jnp.where(jnp.logical_and(j == 0, tblk == nt - 1), 0.0, 1.0)
            xs_r = jnp.where(rows == 0, pltpu.roll(prev_r, 1, 0) * edge, pltpu.roll(x_ref[pl.ds(i, 8), 0:cb], 1, 0))
            xs_i = jnp.where(rows == 0, pltpu.roll(prev_i, 1, 0) * edge, pltpu.roll(x_ref[pl.ds(i, 8), cb:2 * cb], 1, 0))
            sr_acc = sr_acc + lr * xs_r + li * xs_i
            si_acc = si_acc + li * xs_r - lr * xs_i
            return lr[0:1, :], li[0:1, :], sr_acc, si_acc

        cr, ci, sr_acc, si_acc = lax.fori_loop(
            0, nj, step, (carry[0:1, :], carry[1:2, :], acc[:, 0:cb], acc[:, cb:2 * cb]))
        carry[0:1, :] = cr
        carry[1:2, :] = ci
        acc[:, 0:cb] = sr_acc
        acc[:, cb:2 * cb] = si_acc

        @pl.when(tblk == nt - 1)
        def _():
            da_ref[...] = jnp.sum(acc[...], axis=0, keepdims=True)

    blk = pl.BlockSpec((tb, 2 * cb), lambda j, i: (nt - 1 - i, j))
    prev = pl.BlockSpec((8, 2 * cb), lambda j, i: (jnp.maximum((nt - 1 - i) * (tb // 8) - 1, 0), j))
    vec = pl.BlockSpec((1, 2 * cb), lambda j, i: (0, j))
    return pl.pallas_call(
        body, name=name, grid=(SSM_CH // cb, nt), in_specs=[blk, blk, prev, vec], out_specs=[blk, vec],
        out_shape=[jax.ShapeDtypeStruct((t, 2 * SSM_CH), F32), jax.ShapeDtypeStruct((1, 2 * SSM_CH), F32)],
        scratch_shapes=[pltpu.VMEM((8, cb), F32), pltpu.VMEM((8, 2 * cb), F32)],
        compiler_params=_params(("parallel", "arbitrary")))(dx, x, x, a)


def _glu_fwd(yc, u, dvec, wg, bg, name, tr=256):
    t = yc.shape[0]

    def body(yc_ref, u_ref, d_ref, w_ref, b_ref, yl_ref, yb_ref):
        yl = yc_ref[...] + d_ref[...] * u_ref[...]
        yl_ref[...] = yl
        yg, _ = _gelu_and_grad(yl)
        z = jnp.dot(yg.astype(BF16), w_ref[...], preferred_element_type=F32) + b_ref[...]
        yb_ref[...] = (yg * _sigmoid(z)).astype(BF16)

    blk = pl.BlockSpec((tr, SSM_WIDTH), lambda i: (i, 0))
    vec = pl.BlockSpec((1, SSM_WIDTH), lambda i: (0, 0))
    return pl.pallas_call(
        body, name=name, grid=(t // tr,),
        in_specs=[blk, blk, vec, pl.BlockSpec((SSM_WIDTH, SSM_WIDTH), lambda i: (0, 0)), vec],
        out_specs=[blk, blk],
        out_shape=[jax.ShapeDtypeStruct((t, SSM_WIDTH), F32), jax.ShapeDtypeStruct((t, SSM_WIDTH), BF16)],
        compiler_params=_params(("parallel",)))(yc, u, dvec, wg, bg)


def _glu_bwd(yl, u, dvec, wg, bg, dyb, name, tr=256):
    t = yl.shape[0]

    def body(yl_ref, u_ref, d_ref, w_ref, b_ref, dy_ref, dyl_ref, du_ref, dw_ref, db_ref, dd_ref):
        @pl.when(pl.program_id(0) == 0)
        def _():
            dw_ref[...] = jnp.zeros_like(dw_ref)
            db_ref[...] = jnp.zeros_like(db_ref)
            dd_ref[...] = jnp.zeros_like(dd_ref)

        ylv, dyv, wv = yl_ref[...], dy_ref[...].astype(F32), w_ref[...]
        yg, dgelu = _gelu_and_grad(ylv)
        ygb = yg.astype(BF16)
        z = jnp.dot(ygb, wv, preferred_element_type=F32) + b_ref[...]
        sg = _sigmoid(z)
        dz = dyv * yg * sg * (1.0 - sg)
        dzb = dz.astype(BF16)
        dyg = dyv * sg + lax.dot_general(dzb, wv, (((1,), (1,)), ((), ())), preferred_element_type=F32)
        dyl = dyg * dgelu
        dyl_ref[...] = dyl.astype(BF16)
        du_ref[...] = dyl * d_ref[...]
        dw_ref[...] += lax.dot_general(ygb, dzb, (((0,), (0,)), ((), ())), preferred_element_type=F32)
        db_ref[...] += jnp.sum(dz, axis=0, keepdims=True)
        dd_ref[...] += jnp.sum(dyl * u_ref[...], axis=0, keepdims=True)

    blk = pl.BlockSpec((tr, SSM_WIDTH), lambda i: (i, 0))
    vec = pl.BlockSpec((1, SSM_WIDTH), lambda i: (0, 0))
    wsp = pl.BlockSpec((SSM_WIDTH, SSM_WIDTH), lambda i: (0, 0))
    return pl.pallas_call(
        body, name=name, grid=(t // tr,), in_specs=[blk, blk, vec, wsp, vec, blk],
        out_specs=[blk, blk, wsp, vec, vec],
        out_shape=[jax.ShapeDtypeStruct((t, SSM_WIDTH), BF16), jax.ShapeDtypeStruct((t, SSM_WIDTH), F32),
                   jax.ShapeDtypeStruct((SSM_WIDTH, SSM_WIDTH), F32), jax.ShapeDtypeStruct((1, SSM_WIDTH), F32),
                   jax.ShapeDtypeStruct((1, SSM_WIDTH), F32)],
        compiler_params=_params(("arbitrary",)))(yl, u, dvec, wg, bg, dyb)


def _mesh_pos():
    return lax.axis_index("x"), lax.axis_index("y"), lax.axis_index("c")


def _all_gather(arrays, name):
    na = len(arrays)

    def body(*refs):
        ins, outs = refs[:na], refs[na:2 * na]
        send_sems, recv_sems, local_sems = refs[2 * na:]
        x, y, c = _mesh_pos()
        me, sibling = (x, y, c), (x, y, 1 - c)
        chips = [(1 - x, y), (x, 1 - y), (1 - x, 1 - y)]
        waits = []
        for ai in range(na):
            in_ref, out_ref = ins[ai], outs[ai]

            def slot(px, py, pc, out_ref=out_ref):
                return out_ref.at[4 * px + 2 * py + pc]

            def copy(kk, block, to, src=None, ai=ai, slot=slot):
                return pltpu.make_async_remote_copy(
                    src_ref=slot(*block) if src is None else src, dst_ref=slot(*block),
                    send_sem=send_sems.at[ai, kk], recv_sem=recv_sems.at[ai, kk], device_id=to, device_id_type=MESH)

            mine = pltpu.make_async_copy(in_ref, slot(*me), local_sems.at[ai])
            mine.start()
            first = [copy(0, me, sibling, src=in_ref)]
            first += [copy(1 + j, me, (*chip, c), src=in_ref) for j, chip in enumerate(chips)]
            for cp in first:
                cp.start()
            waits.append((copy, mine, first))
        sends = []
        for ai in range(na):
            copy, mine, first = waits[ai]
            passed = [copy(4 + j, (*chip, c), sibling) for j, chip in enumerate(chips)]
            for j, chip in enumerate(chips):
                copy(1 + j, (*chip, c), me).wait_recv()
                passed[j].start()
            sends.append(passed)
        for ai in range(na):
            copy, mine, first = waits[ai]
            copy(0, sibling, me).wait_recv()
            for j, chip in enumerate(chips):
                copy(4 + j, (*chip, 1 - c), me).wait_recv()
            for cp in first + sends[ai]:
                cp.wait_send()
            mine.wait()

    any_spec = pl.BlockSpec(memory_space=pl.ANY)
    return pl.pallas_call(
        body, name=name, in_specs=[any_spec] * na, out_specs=[any_spec] * na,
        out_shape=[jax.ShapeDtypeStruct((N_DEV,) + a.shape, a.dtype) for a in arrays],
        scratch_shapes=[pltpu.SemaphoreType.DMA((na, 7)), pltpu.SemaphoreType.DMA((na, 7)),
                        pltpu.SemaphoreType.DMA((na,))],
        compiler_params=pltpu.CompilerParams(has_side_effects=True))(*arrays)


def _swap_sibling(arrays, name):
    na = len(arrays)
    offs = np.concatenate([[0], np.cumsum([a.shape[1] for a in arrays])]).astype(int)
    rows = int(offs[-1])

    def body(*refs):
        ins, recv_ref = refs[:na], refs[na]
        send_sems, recv_sems = refs[na + 1:]
        x, y, c = _mesh_pos()
        started = []
        for ai in range(na):
            span = pl.ds(int(offs[ai]), arrays[ai].shape[1])
            for k in range(4):
                remote = pltpu.make_async_remote_copy(
                    src_ref=ins[ai].at[2 * k + 1 - c], dst_ref=recv_ref.at[k, span], send_sem=send_sems.at[ai, k],
                    recv_sem=recv_sems.at[ai, k], device_id=(x, y, 1 - c), device_id_type=MESH)
                remote.start()
                started.append(remote)
        for remote in started:
            remote.wait()

    any_spec = pl.BlockSpec(memory_space=pl.ANY)
    return pl.pallas_call(
        body, name=name, in_specs=[any_spec] * na, out_specs=any_spec,
        out_shape=jax.ShapeDtypeStruct((4, rows, PACK_COLS), arrays[0].dtype),
        scratch_shapes=[pltpu.SemaphoreType.DMA((na, 4)), pltpu.SemaphoreType.DMA((na, 4))])(*arrays)


def _exchange_chips(send, name):
    def body(s_ref, o_ref, send_sems, recv_sems):
        x, y, c = _mesh_pos()
        chips = [(1 - x, y), (x, 1 - y), (1 - x, 1 - y)]
        cps = [pltpu.make_async_remote_copy(
            src_ref=s_ref.at[2 * cx + cy], dst_ref=o_ref.at[j], send_sem=send_sems.at[j], recv_sem=recv_sems.at[j],
            device_id=(cx, cy, c), device_id_type=MESH) for j, (cx, cy) in enumerate(chips)]
        for cp in cps:
            cp.start()
        for cp in cps:
            cp.wait()

    any_spec = pl.BlockSpec(memory_space=pl.ANY)
    return pl.pallas_call(
        body, name=name, in_specs=[any_spec], out_specs=any_spec,
        out_shape=jax.ShapeDtypeStruct((3,) + send.shape[1:], send.dtype),
        scratch_shapes=[pltpu.SemaphoreType.DMA((3,)), pltpu.SemaphoreType.DMA((3,))])(send)


def _pair_sum(keep, recv, name, tr=464):
    nchip, rows, cols = keep.shape

    def body(g_ref, r_ref, o_ref):
        o_ref[...] = (g_ref[...].astype(F32) + r_ref[...].astype(F32)).astype(BF16)

    blk = pl.BlockSpec((1, tr, cols), lambda k, i: (k, i, 0))
    return pl.pallas_call(
        body, name=name, grid=(nchip, rows // tr), in_specs=[blk, blk], out_specs=blk,
        out_shape=jax.ShapeDtypeStruct((nchip, rows, cols), BF16),
        compiler_params=_params(("parallel", "parallel")))(keep, recv)


def _chip_sum(own, others, name, tr=464):
    _, rows, cols = own.shape
    chip = (2 * lax.axis_index("x") + lax.axis_index("y")).astype(jnp.int32).reshape(1)

    def body(chip_ref, own_ref, oth_ref, o_ref):
        del chip_ref
        acc = own_ref[0].astype(F32)
        for j in range(3):
            acc = acc + oth_ref[j].astype(F32)
        o_ref[...] = acc

    grid_spec = pltpu.PrefetchScalarGridSpec(
        num_scalar_prefetch=1, grid=(rows // tr,),
        in_specs=[pl.BlockSpec((1, tr, cols), lambda i, chip_ref: (chip_ref[0], i, 0)),
                  pl.BlockSpec((3, tr, cols), lambda i, chip_ref: (0, i, 0))],
        out_specs=pl.BlockSpec((tr, cols), lambda i, chip_ref: (i, 0)))
    return pl.pallas_call(
        body, name=name, grid_spec=grid_spec, out_shape=jax.ShapeDtypeStruct((rows, cols), F32),
        compiler_params=_params(("parallel",)))(chip, own, others)


def _sum_leading(parts, name, tr=464):
    nparts, rows, cols = parts.shape
    tr = tr if rows % tr == 0 else rows

    def body(p_ref, o_ref):
        acc = p_ref[0].astype(F32)
        for i in range(1, nparts):
            acc = acc + p_ref[i].astype(F32)
        o_ref[...] = acc

    return pl.pallas_call(
        body, name=name, grid=(rows // tr,),
        in_specs=[pl.BlockSpec((nparts, tr, cols), lambda i: (0, i, 0))],
        out_specs=pl.BlockSpec((tr, cols), lambda i: (i, 0)), out_shape=jax.ShapeDtypeStruct((rows, cols), F32),
        compiler_params=_params(("parallel",)))(parts)


def _adamw(w, g, m, v, name):
    shape = w.shape
    cols = shape[-1]
    rows = int(np.prod(shape[:-1])) if len(shape) > 1 else 1
    w2, g2, m2, v2 = (a.reshape(rows, cols) for a in (w, g, m, v))
    tr = rows
    for cand in (512, 256, 128, 64, 32, 16, 8):
        if rows % cand == 0 and rows > cand:
            tr = cand
            break
    bc1, bc2 = 1.0 - ADAM_B1 ** ADAM_STEP, 1.0 - ADAM_B2 ** ADAM_STEP

    def body(w_ref, g_ref, m_ref, v_ref, d_ref, nm_ref, nv_ref):
        gv = g_ref[...]
        nm = ADAM_B1 * m_ref[...] + (1.0 - ADAM_B1) * gv
        nv = ADAM_B2 * v_ref[...] + (1.0 - ADAM_B2) * (gv * gv)
        nm_ref[...] = nm
        nv_ref[...] = nv
        d_ref[...] = -ADAM_LR * ((nm / bc1) / (jnp.sqrt(nv / bc2) + ADAM_EPS) + ADAM_WD * w_ref[...])

    blk = pl.BlockSpec((tr, cols), lambda i: (i, 0))
    outs = pl.pallas_call(
        body, name=name, grid=(rows // tr,), in_specs=[blk] * 4, out_specs=[blk] * 3,
        out_shape=[jax.ShapeDtypeStruct((rows, cols), F32)] * 3, compiler_params=_params(("parallel",)))(w2, g2, m2, v2)
    return tuple(o.reshape(shape) for o in outs)


WEIGHT_NAMES = ['norm_mix_g', 'norm_xa_g', 'norm_ffn_g', 'norm_mem_g', 'norm_final_g', 'w_in_ab', 'conv_qkv_a',
                'a_log_a', 'dt_bias_a', 'onorm_g_a', 'ssm_lambda_re', 'ssm_lambda_im', 'ssm_b_re', 'ssm_b_im',
                'ssm_c_re', 'ssm_c_im', 'ssm_d', 'ssm_log_dt', 'w_glu_b', 'b_glu_b', 'w_out_ab', 'pool_w',
                'pool_scale', 'xa_wq', 'xa_wkv', 'xa_wo', 'ffn_w_up', 'ffn_conv', 'ffn_w_down']
BIG_SHARDED = {'w_in_ab': ((1, 1024, 2568), 2), 'w_glu_b': ((1, 512, 512), 1), 'w_out_ab': ((1, 1024, 1024), 1),
               'pool_w': ((1, 4, 256, 256), 2), 'xa_wq': ((2, 1024, 1024), 1), 'xa_wkv': ((2, 1024, 2048), 2),
               'xa_wo': ((2, 1024, 1024), 1), 'ffn_w_up': ((2, 1024, 5632), 2), 'ffn_w_down': ((2, 2816, 1024), 1)}
SMALL_SHARDED = {'conv_qkv_a': ((1, 4, 1536), 2), 'pool_scale': ((1, 1024), 1), 'ffn_conv': ((2, 3, 5632), 2)}
REPLICATED = {'norm_mix_g': (2, 1024), 'norm_xa_g': (2, 1024), 'norm_ffn_g': (2, 1024), 'norm_mem_g': (1024,),
              'norm_final_g': (1024,), 'a_log_a': (1, 4), 'dt_bias_a': (1, 4), 'onorm_g_a': (1, 128),
              'ssm_lambda_re': (1, 32, 64), 'ssm_lambda_im': (1, 32, 64), 'ssm_b_re': (1, 32, 64, 16),
              'ssm_b_im': (1, 32, 64, 16), 'ssm_c_re': (1, 32, 16, 64), 'ssm_c_im': (1, 32, 16, 64),
              'ssm_d': (1, 32, 16), 'ssm_log_dt': (1, 32), 'b_glu_b': (1, 512)}
BIG_ROW_ALIGN = 464 * 8


def _shard_shape(shape, axis):
    return tuple(s // N_DEV if i == axis else s for i, s in enumerate(shape))


def _pad_rows(a, rows):
    return a if a.shape[0] == rows else jnp.concatenate([a, jnp.zeros((rows - a.shape[0],) + a.shape[1:], a.dtype)], 0)


def _round_up(n, m):
    return (n + m - 1) // m * m


def _pack_rows(flat_list, align):
    rows = []
    for a in flat_list:
        n = a.shape[0]
        r = _round_up(n, PACK_COLS) // PACK_COLS
        if n != r * PACK_COLS:
            a = jnp.concatenate([a, jnp.zeros((r * PACK_COLS - n,), a.dtype)])
        rows.append(a.reshape(r, PACK_COLS))
    out = jnp.concatenate(rows, 0)
    return _pad_rows(out, _round_up(out.shape[0], align))


def _row_offsets(sizes):
    offs, r = [], 0
    for n in sizes:
        offs.append(r)
        r += _round_up(n, PACK_COLS) // PACK_COLS
    return offs


def _split_shards(full, axis):
    shape = full.shape
    s = shape[axis] // N_DEV
    a = full.reshape(shape[:axis] + (N_DEV, s) + shape[axis + 1:])
    return jnp.moveaxis(a, axis, 0).reshape(N_DEV, -1)


def _merge_shards(pieces, shape, axis):
    sh = _shard_shape(shape, axis)
    a = pieces.reshape((N_DEV,) + sh)
    a = jnp.moveaxis(a, 0, axis)
    return a.reshape(shape)


_SCAN_NB = SSM_CH // SCAN_CB


def _to_scan_layout(m, axis):
    shape = m.shape
    m = m.reshape(shape[:axis] + (2, _SCAN_NB, SCAN_CB) + shape[axis + 1:])
    return jnp.swapaxes(m, axis, axis + 1).reshape(shape)


def _from_scan_layout(m, axis):
    shape = m.shape
    m = m.reshape(shape[:axis] + (_SCAN_NB, 2, SCAN_CB) + shape[axis + 1:])
    return jnp.swapaxes(m, axis, axis + 1).reshape(shape)


def _s5_discretise(lam_re, lam_im, b_re, b_im, log_dt):
    dt = jnp.exp(log_dt)[:, None]
    mag = jnp.exp(lam_re * dt)
    ang = lam_im * dt
    lb_re, lb_im = mag * jnp.cos(ang), mag * jnp.sin(ang)
    den = lam_re * lam_re + lam_im * lam_im
    nr, ni = lb_re - 1.0, lb_im
    coef_re = (nr * lam_re + ni * lam_im) / den
    coef_im = (ni * lam_re - nr * lam_im) / den
    bb_re = coef_re[..., None] * b_re - coef_im[..., None] * b_im
    bb_im = coef_re[..., None] * b_im + coef_im[..., None] * b_re
    return lb_re, lb_im, bb_re, bb_im


def _s5_matrices(lb_re, lb_im, bb_re, bb_im, c_re, c_im):
    eye = jnp.eye(N_GROUPS, dtype=F32)
    bmat = lambda bb: jnp.einsum('gph,gk->ghkp', bb, eye).reshape(SSM_WIDTH, SSM_CH)
    cmat = lambda cc: jnp.einsum('ghp,gk->gpkh', cc, eye).reshape(SSM_CH, SSM_WIDTH)
    b_in = _to_scan_layout(jnp.concatenate([bmat(bb_re), bmat(bb_im)], axis=1), 1)
    c_out = _to_scan_layout(jnp.concatenate([cmat(c_re), -cmat(c_im)], axis=0), 0)
    a_row = _to_scan_layout(jnp.concatenate([lb_re.reshape(1, SSM_CH), lb_im.reshape(1, SSM_CH)], axis=1), 1)
    return b_in, c_out, a_row


def _s5_matrix_grads(db_in, dc_out, da_row):
    db_nat = _from_scan_layout(db_in, 1)
    dc_nat = _from_scan_layout(dc_out, 0)
    da_nat = _from_scan_layout(da_row, 1)
    eye = jnp.eye(N_GROUPS, dtype=F32)
    bgrad = lambda m: jnp.einsum('ghkp,gk->gph', m.reshape(N_GROUPS, SSM_GROUP, N_GROUPS, SSM_STATE), eye)
    cgrad = lambda m: jnp.einsum('gpkh,gk->ghp', m.reshape(N_GROUPS, SSM_STATE, N_GROUPS, SSM_GROUP), eye)
    dbb_re, dbb_im = bgrad(db_nat[:, :SSM_CH]), bgrad(db_nat[:, SSM_CH:])
    dc_re, dc_im = cgrad(dc_nat[:SSM_CH]), -cgrad(dc_nat[SSM_CH:])
    dlb_re = da_nat[0, :SSM_CH].reshape(N_GROUPS, SSM_STATE)
    dlb_im = da_nat[0, SSM_CH:].reshape(N_GROUPS, SSM_STATE)
    return dlb_re, dlb_im, dbb_re, dbb_im, dc_re, dc_im


def _hybrid_fwd(xn, x, wts, p):
    sv = {}
    hq = _mm(xn, wts['w_qkv_t'], "nt", "l0_in_qkv")
    gate = _mm(xn, wts['w_gate_t'], "nt", "l0_in_gate")
    ba = _mm(xn, wts['w_ba_t'], "nt", "l0_in_ba")
    u = _mm(xn, wts['w_u_t'], "nt", "l0_in_u")
    conv = p['conv_qkv']
    q = _qkv_pre_fwd(hq, conv, 0, 4, True, HEAD_A ** -0.5, "l0_q_pre")
    k = _qkv_pre_fwd(hq, conv, 4, 4, True, 1.0, "l0_k_pre")
    v = _qkv_pre_fwd(hq, conv, 8, 4, False, 1.0, "l0_v_pre")
    gates = _gates_fwd(ba, p['arow'], p['brow'], "l0_gates")
    gb = jnp.stack([gates[:, 0:4].T, gates[:, 4:8].T], axis=-1)
    o, tm_all, s_all = _gdr_fwd(q, k, v, gb, "l0_gdr_fwd")
    y_a = _onorm_fwd(o, gate, p['onorm_g'], "l0_onorm")
    bu = _mm(u, p['b_in'], "nn", "l0_s5_bu")
    xs = _s5_scan_fwd(bu, p['a_row'], "l0_s5_scan")
    yc = _mm(xs, p['c_out'], "nn", "l0_s5_cx")
    yl, y_b = _glu_fwd(yc, u, p['d_row'], wts['w_glu'], p['b_glu'], "l0_glu")
    mixed = jnp.concatenate([y_a, y_b], axis=1)
    x1 = _mm(mixed, wts['w_out'], "nn", "l0_out", res=x)
    sv.update(hq=hq, gate=gate, ba=ba, u=u, q=q, k=k, v=v, gb=gb, o=o, tm=tm_all, s=s_all, xs=xs, yl=yl, mixed=mixed)
    return x1, sv


def _hybrid_bwd(dx1, xn, wts, p, sv):
    gr = {}
    dmixed = _mm(dx1, wts['w_out'], "nt", "l0_out_dx", out_dtype=BF16)
    gr['w_out_ab'] = _mm(sv['mixed'], dx1, "tn", "l0_out_dw", out_dtype=BF16)
    dya, dyb = dmixed[:, :WIDTH_A], dmixed[:, WIDTH_A:]
    dyl, du_direct, gr['w_glu_b'], gr['b_glu_b'], dd = _glu_bwd(
        sv['yl'], sv['u'], p['d_row'], wts['w_glu'], p['b_glu'], dyb, "l0_glu_bwd")
    dxs = _mm(dyl, p['c_out'], "nt", "l0_s5_cx_dx")
    dc_out = _mm(sv['xs'], dyl, "tn", "l0_s5_cx_dw")
    lam, da_row = _s5_scan_bwd(dxs, sv['xs'], p['a_row'], "l0_s5_scan_bwd")
    du = _mm(lam, p['b_in'], "nt", "l0_s5_bu_dx", res=du_direct, out_dtype=BF16)
    db_in = _mm(sv['u'], lam, "tn", "l0_s5_bu_dw")
    gr['s5'] = (db_in, dc_out, da_row, dd)
    do, dgate, gr['onorm_g_a'] = _onorm_bwd(sv['o'], sv['gate'], p['onorm_g'], dya, "l0_onorm_bwd")
    dq, dk, dv, dgb = _gdr_bwd(sv['q'], sv['k'], sv['v'], sv['gb'], sv['tm'], sv['s'], do, "l0_gdr_bwd")
    conv = p['conv_qkv']
    dhq_q, dcw_q = _qkv_pre_bwd(sv['hq'], conv, dq, 0, 4, True, HEAD_A ** -0.5, "l0_q_pre_bwd")
    dhq_k, dcw_k = _qkv_pre_bwd(sv['hq'], conv, dk, 4, 4, True, 1.0, "l0_k_pre_bwd")
    dhq_v, dcw_v = _qkv_pre_bwd(sv['hq'], conv, dv, 8, 4, False, 1.0, "l0_v_pre_bwd")
    gr['conv_qkv_a'] = jnp.concatenate([dcw_q, dcw_k, dcw_v], axis=1)
    dhq = jnp.concatenate([dhq_q, dhq_k, dhq_v], axis=1)
    dgates = jnp.concatenate([dgb[:, :, 0].T, dgb[:, :, 1].T, jnp.zeros((SEQ, LANE - 8), F32)], axis=1)
    dba, da_log, ddt_bias = _gates_bwd(sv['ba'], p['arow'], p['brow'], dgates, "l0_gates_bwd")
    gr['a_log_a'], gr['dt_bias_a'] = da_log[:, 4:8], ddt_bias[:, 4:8]
    dxn = _mm(dhq, wts['w_qkv_t'], "nn", "l0_in_qkv_dx")
    dxn = _mm(dgate, wts['w_gate_t'], "nn", "l0_in_gate_dx", res=dxn)
    dxn = _mm(dba, wts['w_ba_t'], "nn", "l0_in_ba_dx", res=dxn)
    dxn = _mm(du, wts['w_u_t'], "nn", "l0_in_u_dx", res=dxn)
    dw_qkv_t = _mm(dhq, xn, "tn", "l0_in_qkv_dw", out_dtype=BF16)
    dw_gate_t = _mm(dgate, xn, "tn", "l0_in_gate_dw", out_dtype=BF16)
    dw_ba_t = _mm(dba, xn, "tn", "l0_in_ba_dw", out_dtype=BF16)
    dw_u_t = _mm(du, xn, "tn", "l0_in_u_dw", out_dtype=BF16)
    gr['w_in_t'] = jnp.concatenate([dw_qkv_t, dw_gate_t, dw_ba_t[:8], dw_u_t], axis=0)
    return dxn, gr


def _xa_fwd(x1, g, mem_n, wq, wkv_t, wo, tag):
    xq = _rms_fwd(x1, g, BF16, tag + "_norm")
    q = _mm(xq, wq, "nn", tag + "_q", out_dtype=BF16)
    kv = _mm(mem_n, wkv_t, "nt", tag + "_kv", out_dtype=BF16)
    o = _attn_fwd(q, kv, tag + "_attn")
    x2 = _mm(o, wo, "nn", tag + "_o", res=x1)
    return x2, dict(xq=xq, q=q, kv=kv, o=o)


def _xa_bwd(dx2, x1, g, mem_n, wq, wkv_t, wo, sv, tag):
    do = _mm(dx2, wo, "nt", tag + "_o_dx", out_dtype=BF16)
    dwo = _mm(sv['o'], dx2, "tn", tag + "_o_dw", out_dtype=BF16)
    dq, dk, dv = _attn_bwd(sv['q'], sv['kv'], do, tag + "_attn_bwd")
    dkv = jnp.concatenate([dk, dv], axis=1).astype(BF16)
    dxq = _mm(dq, wq, "nt", tag + "_q_dx")
    dwq = _mm(sv['xq'], dq, "tn", tag + "_q_dw", out_dtype=BF16)
    dmem_n = _mm(dkv, wkv_t, "nn", tag + "_kv_dx")
    dwkv_t = _mm(dkv, mem_n, "tn", tag + "_kv_dw", out_dtype=BF16)
    dx1, dg = _rms_bwd(x1, g, dxq, dx2, tag + "_norm_bwd")
    return dx1, dmem_n, dict(wq=dwq, wkv_t=dwkv_t, wo=dwo, g=dg)


def _ffn_fwd(x2, g, w_up_t, conv, w_down, tag):
    xf = _rms_fwd(x2, g, BF16, tag + "_norm")
    h = _mm(xf, w_up_t, "nt", tag + "_up")
    a = _ffn_act_fwd(h, conv, tag + "_act")
    x3 = _mm(a, w_down, "nn", tag + "_down", res=x2)
    return x3, dict(xf=xf, h=h, a=a)


def _ffn_bwd(dx3, x2, g, w_up_t, conv, w_down, sv, tag):
    da = _mm(dx3, w_down, "nt", tag + "_down_dx")
    dw_down = _mm(sv['a'], dx3, "tn", tag + "_down_dw", out_dtype=BF16)
    dh, dconv = _ffn_act_bwd(sv['h'], conv, da, tag + "_act_bwd")
    dxf = _mm(dh, w_up_t, "nn", tag + "_up_dx")
    dw_up_t = _mm(dh, sv['xf'], "tn", tag + "_up_dw", out_dtype=BF16)
    dx2, dg = _rms_bwd(x2, g, dxf, dx3, tag + "_norm_bwd")
    return dx2, dict(w_up_t=dw_up_t, conv=dconv, w_down=dw_down, g=dg)


BIG_NAMES, SMALL_NAMES, REP_NAMES = list(BIG_SHARDED), list(SMALL_SHARDED), list(REPLICATED)
BIG_SIZES = [int(np.prod(_shard_shape(*BIG_SHARDED[n]))) for n in BIG_NAMES]
SMALL_SIZES = [int(np.prod(_shard_shape(*SMALL_SHARDED[n]))) for n in SMALL_NAMES]


PIECES = [('w_in_t', 384), ('w_glu', 32), ('w_out', 128), ('pool_w', 32), ('wq0', 128), ('wq1', 128),
          ('wkv_t0', 256), ('wkv_t1', 256), ('wo0', 128), ('wo1', 128), ('up_t0', 704), ('up_t1', 704),
          ('down0', 352), ('down1', 352)]
PIECE_OFFS = dict(zip([k for k, _ in PIECES], np.concatenate([[0], np.cumsum([r for _, r in PIECES])[:-1]]).tolist()))
W_IN_ROWS = 4 * WIDTH_A + 2 * N_HEADS_A + SSM_WIDTH
W_IN_PIECE = W_IN_ROWS // N_DEV


def _gather_weights(inp):
    bf = lambda a: a.astype(BF16)
    local = {'w_in_t': bf(inp['w_in_ab'][0]).T, 'w_glu': bf(inp['w_glu_b'][0]), 'w_out': bf(inp['w_out_ab'][0]),
             'pool_w': bf(inp['pool_w'][0])}
    for l in range(2):
        local['wq%d' % l] = bf(inp['xa_wq'][l])
        local['wkv_t%d' % l] = bf(inp['xa_wkv'][l]).T
        local['wo%d' % l] = bf(inp['xa_wo'][l])
        local['up_t%d' % l] = bf(inp['ffn_w_up'][l]).T
        local['down%d' % l] = bf(inp['ffn_w_down'][l])
    small_offs = _row_offsets(SMALL_SIZES)
    small_local = _pack_rows([inp[n].reshape(-1) for n in SMALL_NAMES], 8)
    keys = [k for k, _ in PIECES]
    gathered = _all_gather([local[k] for k in keys] + [small_local], "gather_weights")
    got = dict(zip(keys, gathered[:-1]))
    small_all = gathered[-1]
    rows = lambda a: a.reshape(N_DEV * a.shape[1], a.shape[2])
    full = {'w_in_t': rows(got['w_in_t']), 'w_glu': rows(got['w_glu']), 'w_out': rows(got['w_out']),
            'pool_w': jnp.swapaxes(got['pool_w'], 0, 1).reshape(len(POOL_WINDOWS), POOL_GROUP, POOL_GROUP)}
    for name in ('wq', 'wkv_t', 'wo', 'up_t', 'down'):
        full[name] = [rows(got[name + '0']), rows(got[name + '1'])]
    for n, off, size in zip(SMALL_NAMES, small_offs, SMALL_SIZES):
        r = _round_up(size, PACK_COLS) // PACK_COLS
        full[n] = _merge_shards(small_all[:, off:off + r].reshape(N_DEV, -1)[:, :size], *SMALL_SHARDED[n])
    return full


def _local_step(inp, full):
    f32_of = lambda n: inp[n].astype(F32)
    w_in_t = full['w_in_t']
    wts0 = dict(w_qkv_t=w_in_t[:3 * WIDTH_A], w_gate_t=w_in_t[3 * WIDTH_A:4 * WIDTH_A],
                w_ba_t=jnp.concatenate([w_in_t[4 * WIDTH_A:4 * WIDTH_A + 8], jnp.zeros((LANE - 8, D_MODEL), BF16)], 0),
                w_u_t=w_in_t[4 * WIDTH_A + 8:], w_glu=full['w_glu'], w_out=full['w_out'])
    lb_disc, disc_vjp = jax.vjp(_s5_discretise, f32_of('ssm_lambda_re')[0], f32_of('ssm_lambda_im')[0],
                                f32_of('ssm_b_re')[0], f32_of('ssm_b_im')[0], f32_of('ssm_log_dt')[0])
    b_in, c_out, a_row = _s5_matrices(*lb_disc, f32_of('ssm_c_re')[0], f32_of('ssm_c_im')[0])
    zeros4 = jnp.zeros((1, 4), F32)
    p0 = dict(conv_qkv=full['conv_qkv_a'][0], onorm_g=f32_of('onorm_g_a'),
              arow=jnp.concatenate([zeros4, f32_of('a_log_a'), jnp.zeros((1, LANE - 8), F32)], 1),
              brow=jnp.concatenate([zeros4, f32_of('dt_bias_a'), jnp.zeros((1, LANE - 8), F32)], 1),
              b_in=b_in.astype(BF16), c_out=c_out.astype(BF16), a_row=a_row,
              d_row=f32_of('ssm_d').reshape(1, SSM_WIDTH), b_glu=f32_of('b_glu_b'))

    x0 = inp['x'][0]
    mem_n = _rms_fwd(inp['mem'][0], inp['norm_mem_g'], BF16, "mem_norm")
    xn0 = _rms_fwd(x0, inp['norm_mix_g'][0], BF16, "l0_mix_norm")
    x1, sv_mix0 = _hybrid_fwd(xn0, x0, wts0, p0)
    x2, sv_xa0 = _xa_fwd(x1, inp['norm_xa_g'][0], mem_n, full['wq'][0], full['wkv_t'][0], full['wo'][0], "l0_xa")
    x3, sv_ffn0 = _ffn_fwd(x2, inp['norm_ffn_g'][0], full['up_t'][0], full['ffn_conv'][0], full['down'][0], "l0_ffn")
    xn1 = _rms_fwd(x3, inp['norm_mix_g'][1], F32, "l1_mix_norm")
    x4 = _pool_fwd(xn1, full['pool_w'], full['pool_scale'], x3, "l1_pool")
    x5, sv_xa1 = _xa_fwd(x4, inp['norm_xa_g'][1], mem_n, full['wq'][1], full['wkv_t'][1], full['wo'][1], "l1_xa")
    x6, sv_ffn1 = _ffn_fwd(x5, inp['norm_ffn_g'][1], full['up_t'][1], full['ffn_conv'][1], full['down'][1], "l1_ffn")
    loss_part, dx6, dg_final = _loss_head(x6, inp['norm_final_g'], inp['loss_target'][0], "loss_head")

    dx5, g_ffn1 = _ffn_bwd(dx6, x5, inp['norm_ffn_g'][1], full['up_t'][1], full['ffn_conv'][1], full['down'][1], sv_ffn1, "l1_ffn")
    dx4, dmem1, g_xa1 = _xa_bwd(dx5, x4, inp['norm_xa_g'][1], mem_n, full['wq'][1], full['wkv_t'][1], full['wo'][1], sv_xa1, "l1_xa")
    dxn1, dpool_w, dpool_scale = _pool_bwd(xn1, full['pool_w'], full['pool_scale'], dx4, "l1_pool_bwd")
    dx3, dg_mix1 = _rms_bwd(x3, inp['norm_mix_g'][1], dxn1, dx4, "l1_mix_norm_bwd")
    dx2, g_ffn0 = _ffn_bwd(dx3, x2, inp['norm_ffn_g'][0], full['up_t'][0], full['ffn_conv'][0], full['down'][0], sv_ffn0, "l0_ffn")
    dx1, dmem0, g_xa0 = _xa_bwd(dx2, x1, inp['norm_xa_g'][0], mem_n, full['wq'][0], full['wkv_t'][0], full['wo'][0], sv_xa0, "l0_xa")
    dxn0, g_mix0 = _hybrid_bwd(dx1, xn0, wts0, p0, sv_mix0)
    grad_x, dg_mix0 = _rms_bwd(x0, inp['norm_mix_g'][0], dxn0, dx1, "l0_mix_norm_bwd")
    _, dg_mem = _rms_bwd(inp['mem'][0], inp['norm_mem_g'], dmem0 + dmem1, None, "mem_norm_bwd")

    db_in, dc_out, da_row, dd = g_mix0['s5']
    dlb_re, dlb_im, dbb_re, dbb_im, dc_re, dc_im = _s5_matrix_grads(db_in, dc_out, da_row)
    dlam_re, dlam_im, dbr, dbi, dlog_dt = disc_vjp((dlb_re, dlb_im, dbb_re, dbb_im))

    pieces = lambda a: a.reshape(N_DEV, a.shape[0] // N_DEV, a.shape[1])
    w_in_pieces = pieces(g_mix0['w_in_t'])
    w_in_pieces = jnp.concatenate(
        [w_in_pieces, jnp.zeros((N_DEV, PIECES[0][1] - W_IN_PIECE, D_MODEL), BF16)], axis=1)
    pool_pieces = jnp.swapaxes(dpool_w.astype(BF16).reshape(len(POOL_WINDOWS), N_DEV, -1, POOL_GROUP), 0, 1)
    big_grads = {
        'w_in_t': w_in_pieces, 'w_glu': g_mix0['w_glu_b'].astype(BF16).reshape(N_DEV, -1, PACK_COLS),
        'w_out': pieces(g_mix0['w_out_ab']), 'pool_w': pool_pieces.reshape(N_DEV, -1, PACK_COLS)}
    for l, (gx, gf) in enumerate(((g_xa0, g_ffn0), (g_xa1, g_ffn1))):
        big_grads['wq%d' % l], big_grads['wkv_t%d' % l] = pieces(gx['wq']), pieces(gx['wkv_t'])
        big_grads['wo%d' % l] = pieces(gx['wo'])
        big_grads['up_t%d' % l], big_grads['down%d' % l] = pieces(gf['w_up_t']), pieces(gf['w_down'])
    rep_grads = {
        'norm_mix_g': jnp.concatenate([dg_mix0, dg_mix1], 0), 'norm_xa_g': jnp.concatenate([g_xa0['g'], g_xa1['g']], 0),
        'norm_ffn_g': jnp.concatenate([g_ffn0['g'], g_ffn1['g']], 0), 'norm_mem_g': dg_mem.reshape(-1),
        'norm_final_g': dg_final.reshape(-1), 'a_log_a': g_mix0['a_log_a'], 'dt_bias_a': g_mix0['dt_bias_a'],
        'onorm_g_a': g_mix0['onorm_g_a'], 'ssm_lambda_re': dlam_re[None], 'ssm_lambda_im': dlam_im[None],
        'ssm_b_re': dbr[None], 'ssm_b_im': dbi[None], 'ssm_c_re': dc_re[None], 'ssm_c_im': dc_im[None],
        'ssm_d': dd.reshape(1, N_GROUPS, SSM_GROUP), 'ssm_log_dt': dlog_dt[None], 'b_glu_b': g_mix0['b_glu_b']}
    small_grads = {'conv_qkv_a': g_mix0['conv_qkv_a'][None], 'pool_scale': dpool_scale,
                   'ffn_conv': jnp.stack([g_ffn0['conv'], g_ffn1['conv']])}
    return loss_part, grad_x, big_grads, rep_grads, small_grads


def _reduce_gradients(big_grads, rep_grads, small_grads):
    for key, rows in PIECES:
        assert big_grads[key].shape == (N_DEV, rows, PACK_COLS) and big_grads[key].dtype == BF16, key
    piece_list = [big_grads[k] for k, _ in PIECES]
    core = lax.axis_index("c")
    keep = jnp.concatenate([lax.dynamic_index_in_dim(a.reshape(4, 2, a.shape[1], PACK_COLS), core, 1, keepdims=False)
                            for a in piece_list], axis=1)
    from_sibling = _swap_sibling(piece_list, "grads_to_sibling")
    chip_sums = _pair_sum(keep, from_sibling, "grads_pair_sum")
    from_chips = _exchange_chips(chip_sums, "grads_to_chips")
    big_reduced = _chip_sum(chip_sums, from_chips, "grads_chip_sum")

    misc_list = [rep_grads[n].astype(F32).reshape(-1) for n in REP_NAMES] + \
                [small_grads[n].astype(F32).reshape(-1) for n in SMALL_NAMES]
    misc_sizes = [int(a.shape[0]) for a in misc_list]
    misc_local = _pack_rows(misc_list, 8)
    (misc_all,) = _all_gather([misc_local], "gather_small_grads")
    misc_sum = _sum_leading(misc_all, "small_grads_sum")
    return big_reduced, misc_sum, misc_sizes


def _update(inp, loss_part, grad_x, big_reduced, misc_sum, misc_sizes):
    big_names, small_names, rep_names = BIG_NAMES, SMALL_NAMES, REP_NAMES
    misc_offs = _row_offsets(misc_sizes)
    dev = 4 * lax.axis_index("x") + 2 * lax.axis_index("y") + lax.axis_index("c")
    piece_rows = dict(PIECES)
    piece = lambda key, rows=None: big_reduced[PIECE_OFFS[key]:PIECE_OFFS[key] + (rows or piece_rows[key])]
    both = lambda name, fn: jnp.stack([fn(piece(name + '0')), fn(piece(name + '1'))])
    ident, transpose = (lambda a: a), (lambda a: a.T)
    grads = {'w_in_ab': piece('w_in_t', W_IN_PIECE).T[None], 'w_glu_b': piece('w_glu').reshape(inp['w_glu_b'].shape),
             'w_out_ab': piece('w_out')[None], 'pool_w': piece('pool_w').reshape(inp['pool_w'].shape),
             'xa_wq': both('wq', ident), 'xa_wkv': both('wkv_t', transpose), 'xa_wo': both('wo', ident),
             'ffn_w_up': both('up_t', transpose), 'ffn_w_down': both('down', ident)}
    for n, off, size in zip(rep_names + small_names, misc_offs, misc_sizes):
        flat = misc_sum[off:off + _round_up(size, PACK_COLS) // PACK_COLS].reshape(-1)[:size]
        if n in REPLICATED:
            grads[n] = flat.reshape(inp[n].shape)
        else:
            shape, axis = SMALL_SHARDED[n]
            grads[n] = lax.dynamic_index_in_dim(_split_shards(flat.reshape(shape), axis), dev, 0, keepdims=False
                                                ).reshape(inp[n].shape)
    tiny_names = [n for n in WEIGHT_NAMES if n not in BIG_SHARDED]
    upd = {}
    for n in big_names:
        upd[n] = _adamw(inp[n], grads[n], inp['m_' + n], inp['v_' + n], "adamw_" + n)
    tiny_sizes = [int(np.prod(inp[n].shape)) for n in tiny_names]
    tiny_offs = _row_offsets(tiny_sizes)
    packs = [_pack_rows([src(n).astype(F32).reshape(-1) for n in tiny_names], 8)
             for src in (lambda n: inp[n], lambda n: grads[n], lambda n: inp['m_' + n], lambda n: inp['v_' + n])]
    tiny_out = _adamw(*packs, "adamw_small")
    for n, off, size in zip(tiny_names, tiny_offs, tiny_sizes):
        r = _round_up(size, PACK_COLS) // PACK_COLS
        upd[n] = tuple(o[off:off + r].reshape(-1)[:size].reshape(inp[n].shape) for o in tiny_out)

    loss = lax.psum(loss_part[0, 0], ("x", "y", "c"))
    outs = [loss, grad_x[None]]
    outs += [grads[n] for n in WEIGHT_NAMES]
    for i in range(3):
        outs += [upd[n][i] for n in WEIGHT_NAMES]
    return tuple(outs)


def _step(inp):
    full = _gather_weights(inp)
    loss_part, grad_x, big_grads, rep_grads, small_grads = _local_step(inp, full)
    big_reduced, misc_sum, misc_sizes = _reduce_gradients(big_grads, rep_grads, small_grads)
    return _update(inp, loss_part, grad_x, big_reduced, misc_sum, misc_sizes)


INPUT_NAMES = (['x', 'mem'] + WEIGHT_NAMES + ['loss_target'] + ['m_' + n for n in WEIGHT_NAMES]
               + ['v_' + n for n in WEIGHT_NAMES])


def kernel(x, mem, norm_mix_g, norm_xa_g, norm_ffn_g, norm_mem_g, norm_final_g, w_in_ab, conv_qkv_a, a_log_a, dt_bias_a, onorm_g_a, ssm_lambda_re, ssm_lambda_im, ssm_b_re, ssm_b_im, ssm_c_re, ssm_c_im, ssm_d, ssm_log_dt, w_glu_b, b_glu_b, w_out_ab, pool_w, pool_scale, xa_wq, xa_wkv, xa_wo, ffn_w_up, ffn_conv, ffn_w_down, loss_target, m_norm_mix_g, m_norm_xa_g, m_norm_ffn_g, m_norm_mem_g, m_norm_final_g, m_w_in_ab, m_conv_qkv_a, m_a_log_a, m_dt_bias_a, m_onorm_g_a, m_ssm_lambda_re, m_ssm_lambda_im, m_ssm_b_re, m_ssm_b_im, m_ssm_c_re, m_ssm_c_im, m_ssm_d, m_ssm_log_dt, m_w_glu_b, m_b_glu_b, m_w_out_ab, m_pool_w, m_pool_scale, m_xa_wq, m_xa_wkv, m_xa_wo, m_ffn_w_up, m_ffn_conv, m_ffn_w_down, v_norm_mix_g, v_norm_xa_g, v_norm_ffn_g, v_norm_mem_g, v_norm_final_g, v_w_in_ab, v_conv_qkv_a, v_a_log_a, v_dt_bias_a, v_onorm_g_a, v_ssm_lambda_re, v_ssm_lambda_im, v_ssm_b_re, v_ssm_b_im, v_ssm_c_re, v_ssm_c_im, v_ssm_d, v_ssm_log_dt, v_w_glu_b, v_b_glu_b, v_w_out_ab, v_pool_w, v_pool_scale, v_xa_wq, v_xa_wkv, v_xa_wo, v_ffn_w_up, v_ffn_conv, v_ffn_w_down):
    args = (x, mem, norm_mix_g, norm_xa_g, norm_ffn_g, norm_mem_g, norm_final_g, w_in_ab, conv_qkv_a, a_log_a, dt_bias_a, onorm_g_a, ssm_lambda_re, ssm_lambda_im, ssm_b_re, ssm_b_im, ssm_c_re, ssm_c_im, ssm_d, ssm_log_dt, w_glu_b, b_glu_b, w_out_ab, pool_w, pool_scale, xa_wq, xa_wkv, xa_wo, ffn_w_up, ffn_conv, ffn_w_down, loss_target, m_norm_mix_g, m_norm_xa_g, m_norm_ffn_g, m_norm_mem_g, m_norm_final_g, m_w_in_ab, m_conv_qkv_a, m_a_log_a, m_dt_bias_a, m_onorm_g_a, m_ssm_lambda_re, m_ssm_lambda_im, m_ssm_b_re, m_ssm_b_im, m_ssm_c_re, m_ssm_c_im, m_ssm_d, m_ssm_log_dt, m_w_glu_b, m_b_glu_b, m_w_out_ab, m_pool_w, m_pool_scale, m_xa_wq, m_xa_wkv, m_xa_wo, m_ffn_w_up, m_ffn_conv, m_ffn_w_down, v_norm_mix_g, v_norm_xa_g, v_norm_ffn_g, v_norm_mem_g, v_norm_final_g, v_w_in_ab, v_conv_qkv_a, v_a_log_a, v_dt_bias_a, v_onorm_g_a, v_ssm_lambda_re, v_ssm_lambda_im, v_ssm_b_re, v_ssm_b_im, v_ssm_c_re, v_ssm_c_im, v_ssm_d, v_ssm_log_dt, v_w_glu_b, v_b_glu_b, v_w_out_ab, v_pool_w, v_pool_scale, v_xa_wq, v_xa_wkv, v_xa_wo, v_ffn_w_up, v_ffn_conv, v_ffn_w_down)
    return _step(dict(zip(INPUT_NAMES, args)))
```

```python
import functools
import math

import numpy as np
import jax
import jax.numpy as jnp
from jax import lax
from jax.experimental import pallas as pl
from jax.experimental.pallas import tpu as pltpu

F32, BF16 = jnp.float32, jnp.bfloat16
HIGH, HIGHEST = lax.Precision.HIGH, lax.Precision.HIGHEST
MESH = pl.DeviceIdType.MESH

N_DEV = 8
SEQ, D_MODEL, MEM_LEN = 2048, 1024, 256
WIDTH_A, N_HEADS_A, HEAD_A, CONV_A = 512, 4, 128, 4
GDR_CHUNK = 128
SSM_WIDTH, SSM_GROUP, N_GROUPS, SSM_STATE = 512, 16, 32, 64
SSM_CH = N_GROUPS * SSM_STATE
SCAN_CB = 512
POOL_WINDOWS = (2, 4, 8, 16)
POOL_GROUP = 256
N_HEADS_X, HEAD_X = 4, 256
D_FF, CONV_FFN = 2816, 3
RMS_EPS = 1e-6
ADAM_LR, ADAM_B1, ADAM_B2, ADAM_EPS, ADAM_WD, ADAM_STEP = 0.001, 0.9, 0.999, 1e-08, 0.01, 10
LANE = 128
PACK_COLS = 1024
VMEM_LIMIT_BYTES = 56 * 1024 * 1024


def _params(sem=None):
    return pltpu.CompilerParams(dimension_semantics=sem, vmem_limit_bytes=VMEM_LIMIT_BYTES)


def _tile(dim, pref):
    best = None
    for t in range(LANE, min(dim, pref) + 1, LANE):
        if dim % t == 0:
            best = t
    return best if best is not None else dim


MM_VMEM_BUDGET = 40 * 1024 * 1024


def _mm_tiles(m, n, k, a_bytes, b_bytes, o_bytes, r_bytes):
    for tk in (k, _tile(k, 2048), _tile(k, 1024), _tile(k, 512)):
        for tm, tn in ((1024, 1536), (1024, 1024), (1024, 512), (512, 512), (256, 512), (256, 256)):
            tm, tn = _tile(m, tm), _tile(n, tn)
            acc = 0 if tk == k else tm * tn * 4
            need = 2 * (tm * tk * a_bytes + tk * tn * b_bytes + tm * tn * (o_bytes + r_bytes)) + acc
            if need <= MM_VMEM_BUDGET:
                return tm, tn, tk
    raise ValueError("no matmul tiling fits VMEM")


def _mm(a, b, mode, name, out_dtype=F32, res=None):
    if mode == "nn":
        (m, k), n = a.shape, b.shape[1]
    elif mode == "nt":
        (m, k), n = a.shape, b.shape[0]
    else:
        (k, m), n = a.shape, b.shape[1]
    tm, tn, tk = _mm_tiles(m, n, k, a.dtype.itemsize, b.dtype.itemsize, jnp.dtype(out_dtype).itemsize,
                           0 if res is None else res.dtype.itemsize)
    nk = k // tk
    dims = {"nn": ((1,), (0,)), "nt": ((1,), (1,)), "tn": ((0,), (0,))}[mode]

    def body(*refs):
        if res is None:
            a_ref, b_ref, o_ref = refs[:3]
            r_ref = None
        else:
            a_ref, b_ref, r_ref, o_ref = refs[:4]
        part = lax.dot_general(a_ref[...].astype(BF16), b_ref[...].astype(BF16), (dims, ((), ())),
                               preferred_element_type=F32)

        def finish(out):
            if r_ref is not None:
                out = out + r_ref[...].astype(F32)
            o_ref[...] = out.astype(out_dtype)

        if nk == 1:
            finish(part)
            return
        acc = refs[-1]
        kk = pl.program_id(2)

        @pl.when(kk == 0)
        def _():
            acc[...] = part

        @pl.when(kk > 0)
        def _():
            acc[...] += part

        @pl.when(kk == nk - 1)
        def _():
            finish(acc[...])

    a_spec = (pl.BlockSpec((tk, tm), lambda i, j, q: (q, i)) if mode == "tn"
              else pl.BlockSpec((tm, tk), lambda i, j, q: (i, q)))
    b_spec = (pl.BlockSpec((tn, tk), lambda i, j, q: (j, q)) if mode == "nt"
              else pl.BlockSpec((tk, tn), lambda i, j, q: (q, j)))
    o_spec = pl.BlockSpec((tm, tn), lambda i, j, q: (i, j))
    in_specs, args = [a_spec, b_spec], [a, b]
    if res is not None:
        in_specs.append(o_spec)
        args.append(res)
    return pl.pallas_call(
        body, name=name, grid=(m // tm, n // tn, nk), in_specs=in_specs, out_specs=o_spec,
        out_shape=jax.ShapeDtypeStruct((m, n), out_dtype),
        scratch_shapes=[] if nk == 1 else [pltpu.VMEM((tm, tn), F32)],
        compiler_params=_params(("parallel", "parallel", "arbitrary")))(*args)


def _rms_fwd(x, g, out_dtype, name, tr=256):
    rows, d = x.shape

    def body(x_ref, g_ref, o_ref):
        xv = x_ref[...]
        r = lax.rsqrt(jnp.mean(xv * xv, axis=-1, keepdims=True) + RMS_EPS)
        o_ref[...] = (xv * r * g_ref[...]).astype(out_dtype)

    return pl.pallas_call(
        body, name=name, grid=(rows // tr,),
        in_specs=[pl.BlockSpec((tr, d), lambda i: (i, 0)), pl.BlockSpec((1, d), lambda i: (0, 0))],
        out_specs=pl.BlockSpec((tr, d), lambda i: (i, 0)), out_shape=jax.ShapeDtypeStruct((rows, d), out_dtype),
        compiler_params=_params(("parallel",)))(x, g.reshape(1, d))


def _rms_bwd(x, g, dy, dres, name, tr=256):
    rows, d = x.shape

    def body(*refs):
        if dres is None:
            x_ref, g_ref, dy_ref, dx_ref, dg_ref = refs
            r_ref = None
        else:
            x_ref, g_ref, dy_ref, r_ref, dx_ref, dg_ref = refs

        @pl.when(pl.program_id(0) == 0)
        def _():
            dg_ref[...] = jnp.zeros_like(dg_ref)

        xv, dyv = x_ref[...], dy_ref[...].astype(F32)
        r = lax.rsqrt(jnp.mean(xv * xv, axis=-1, keepdims=True) + RMS_EPS)
        xh = xv * r
        dyg = dyv * g_ref[...]
        dx = r * (dyg - xh * jnp.mean(dyg * xh, axis=-1, keepdims=True))
        if r_ref is not None:
            dx = dx + r_ref[...]
        dx_ref[...] = dx
        dg_ref[...] += jnp.sum(dyv * xh, axis=0, keepdims=True)

    blk = pl.BlockSpec((tr, d), lambda i: (i, 0))
    vec = pl.BlockSpec((1, d), lambda i: (0, 0))
    in_specs, args = [blk, vec, blk], [x, g.reshape(1, d), dy]
    if dres is not None:
        in_specs.append(blk)
        args.append(dres)
    return pl.pallas_call(
        body, name=name, grid=(rows // tr,), in_specs=in_specs, out_specs=[blk, vec],
        out_shape=[jax.ShapeDtypeStruct((rows, d), F32), jax.ShapeDtypeStruct((1, d), F32)],
        compiler_params=_params(("arbitrary",)))(*args)


def _loss_head(x, g, target, name, tr=256):
    rows, d = x.shape

    def body(x_ref, g_ref, t_ref, loss_ref, dx_ref, dg_ref):
        @pl.when(pl.program_id(0) == 0)
        def _():
            dg_ref[...] = jnp.zeros_like(dg_ref)
            loss_ref[...] = jnp.zeros_like(loss_ref)

        xv = x_ref[...]
        r = lax.rsqrt(jnp.mean(xv * xv, axis=-1, keepdims=True) + RMS_EPS)
        xh = xv * r
        err = xh * g_ref[...] - t_ref[...]
        loss_ref[...] += 0.5 * jnp.sum(jnp.mean(err * err, axis=-1, keepdims=True), keepdims=True)
        dyv = err * (1.0 / d)
        dyg = dyv * g_ref[...]
        dx_ref[...] = r * (dyg - xh * jnp.mean(dyg * xh, axis=-1, keepdims=True))
        dg_ref[...] += jnp.sum(dyv * xh, axis=0, keepdims=True)

    blk = pl.BlockSpec((tr, d), lambda i: (i, 0))
    vec = pl.BlockSpec((1, d), lambda i: (0, 0))
    return pl.pallas_call(
        body, name=name, grid=(rows // tr,), in_specs=[blk, vec, blk],
        out_specs=[pl.BlockSpec((1, 1), lambda i: (0, 0)), blk, vec],
        out_shape=[jax.ShapeDtypeStruct((1, 1), F32), jax.ShapeDtypeStruct((rows, d), F32),
                   jax.ShapeDtypeStruct((1, d), F32)],
        compiler_params=_params(("arbitrary",)))(x, g.reshape(1, d), target)


def _shift_down(x, s):
    rows = lax.broadcasted_iota(jnp.int32, x.shape, 0)
    return jnp.where(rows >= s, pltpu.roll(x, s, 0), 0.0)


def _shift_up(x, s):
    n = x.shape[0]
    rows = lax.broadcasted_iota(jnp.int32, x.shape, 0)
    return jnp.where(rows < n - s, pltpu.roll(x, n - s, 0), 0.0)


def _sigmoid(x):
    return 1.0 / (1.0 + jnp.exp(-x))


def _silu_and_grad(x):
    s = _sigmoid(x)
    return x * s, s * (1.0 + x * (1.0 - s))


_GELU_C0, _GELU_C1 = math.sqrt(2.0 / math.pi), 0.044715


def _gelu_and_grad(x):
    th = jnp.tanh(_GELU_C0 * (x + _GELU_C1 * x * x * x))
    y = 0.5 * x * (1.0 + th)
    dy = 0.5 * (1.0 + th) + 0.5 * x * (1.0 - th * th) * _GELU_C0 * (1.0 + 3.0 * _GELU_C1 * x * x)
    return y, dy


def _ffn_act_fwd(h, w, name, tc=256):
    t = h.shape[0]
    nb = D_FF // tc

    def body(hg_ref, hv_ref, wg_ref, wv_ref, a_ref):
        def conv(x, wr):
            return wr[2:3, :] * x + wr[1:2, :] * _shift_down(x, 1) + wr[0:1, :] * _shift_down(x, 2)

        cg = conv(hg_ref[...], wg_ref[...])
        cv = conv(hv_ref[...], wv_ref[...])
        a_ref[...] = (cg * _sigmoid(cg) * cv).astype(BF16)

    return pl.pallas_call(
        body, name=name, grid=(nb,),
        in_specs=[pl.BlockSpec((t, tc), lambda j: (0, j)), pl.BlockSpec((t, tc), lambda j: (0, j + nb)),
                  pl.BlockSpec((CONV_FFN, tc), lambda j: (0, j)), pl.BlockSpec((CONV_FFN, tc), lambda j: (0, j + nb))],
        out_specs=pl.BlockSpec((t, tc), lambda j: (0, j)), out_shape=jax.ShapeDtypeStruct((t, D_FF), BF16),
        compiler_params=_params(("parallel",)))(h, h, w, w)


def _ffn_act_bwd(h, w, da, name, tc=256):
    t = h.shape[0]
    nb = D_FF // tc

    def body(hg_ref, hv_ref, wg_ref, wv_ref, da_ref, dhg_ref, dhv_ref, dwg_ref, dwv_ref):
        hg, hv, wg, wv = hg_ref[...], hv_ref[...], wg_ref[...], wv_ref[...]
        hg1, hg2, hv1, hv2 = _shift_down(hg, 1), _shift_down(hg, 2), _shift_down(hv, 1), _shift_down(hv, 2)
        cg = wg[2:3, :] * hg + wg[1:2, :] * hg1 + wg[0:1, :] * hg2
        cv = wv[2:3, :] * hv + wv[1:2, :] * hv1 + wv[0:1, :] * hv2
        sg, dsg = _silu_and_grad(cg)
        dav = da_ref[...].astype(F32)
        dcv = dav * sg
        dcg = dav * cv * dsg

        def conv_t(dc, wr):
            return wr[2:3, :] * dc + wr[1:2, :] * _shift_up(dc, 1) + wr[0:1, :] * _shift_up(dc, 2)

        dhg_ref[...] = conv_t(dcg, wg).astype(BF16)
        dhv_ref[...] = conv_t(dcv, wv).astype(BF16)
        dwg_ref[0:1, :] = jnp.sum(dcg * hg2, axis=0, keepdims=True)
        dwg_ref[1:2, :] = jnp.sum(dcg * hg1, axis=0, keepdims=True)
        dwg_ref[2:3, :] = jnp.sum(dcg * hg, axis=0, keepdims=True)
        dwv_ref[0:1, :] = jnp.sum(dcv * hv2, axis=0, keepdims=True)
        dwv_ref[1:2, :] = jnp.sum(dcv * hv1, axis=0, keepdims=True)
        dwv_ref[2:3, :] = jnp.sum(dcv * hv, axis=0, keepdims=True)

    big = lambda off: pl.BlockSpec((t, tc), lambda j: (0, j + off))
    small = lambda off: pl.BlockSpec((CONV_FFN, tc), lambda j: (0, j + off))
    dhg, dhv, dwg, dwv = pl.pallas_call(
        body, name=name, grid=(nb,),
        in_specs=[big(0), big(nb), small(0), small(nb), big(0)],
        out_specs=[big(0), big(0), small(0), small(0)],
        out_shape=[jax.ShapeDtypeStruct((t, D_FF), BF16), jax.ShapeDtypeStruct((t, D_FF), BF16),
                   jax.ShapeDtypeStruct((CONV_FFN, D_FF), F32), jax.ShapeDtypeStruct((CONV_FFN, D_FF), F32)],
        compiler_params=_params(("parallel",)))(h, h, w, w, da)
    return jnp.concatenate([dhg, dhv], axis=1), jnp.concatenate([dwg, dwv], axis=1)


def _attn_probs(q, k):
    s = lax.dot_general(q.astype(BF16), k.astype(BF16), (((1,), (1,)), ((), ())),
                        preferred_element_type=F32) * (HEAD_X ** -0.5)
    s = s - jnp.max(s, axis=-1, keepdims=True)
    p = jnp.exp(s)
    return p / jnp.sum(p, axis=-1, keepdims=True)


def _attn_fwd(q, kv, name, tq=512):
    t = q.shape[0]

    def body(q_ref, k_ref, v_ref, o_ref):
        p = _attn_probs(q_ref[...], k_ref[...])
        o_ref[...] = jnp.dot(p.astype(BF16), v_ref[...].astype(BF16), preferred_element_type=F32).astype(BF16)

    return pl.pallas_call(
        body, name=name, grid=(N_HEADS_X, t // tq),
        in_specs=[pl.BlockSpec((tq, HEAD_X), lambda h, i: (i, h)),
                  pl.BlockSpec((MEM_LEN, HEAD_X), lambda h, i: (0, h)),
                  pl.BlockSpec((MEM_LEN, HEAD_X), lambda h, i: (0, h + N_HEADS_X))],
        out_specs=pl.BlockSpec((tq, HEAD_X), lambda h, i: (i, h)),
        out_shape=jax.ShapeDtypeStruct((t, N_HEADS_X * HEAD_X), BF16),
        compiler_params=_params(("parallel", "parallel")))(q, kv, kv)


def _attn_bwd(q, kv, do, name, tq=512):
    t = q.shape[0]

    def body(q_ref, k_ref, v_ref, do_ref, dq_ref, dk_ref, dv_ref):
        @pl.when(pl.program_id(1) == 0)
        def _():
            dk_ref[...] = jnp.zeros_like(dk_ref)
            dv_ref[...] = jnp.zeros_like(dv_ref)

        qb, kb, vb, dob = (r[...].astype(BF16) for r in (q_ref, k_ref, v_ref, do_ref))
        p = _attn_probs(qb, kb)
        dp = lax.dot_general(dob, vb, (((1,), (1,)), ((), ())), preferred_element_type=F32)
        ds = p * (dp - jnp.sum(dp * p, axis=-1, keepdims=True)) * (HEAD_X ** -0.5)
        dsb = ds.astype(BF16)
        dq_ref[...] = jnp.dot(dsb, kb, preferred_element_type=F32).astype(BF16)
        dk_ref[...] += lax.dot_general(dsb, qb, (((0,), (0,)), ((), ())), preferred_element_type=F32)
        dv_ref[...] += lax.dot_general(p.astype(BF16), dob, (((0,), (0,)), ((), ())), preferred_element_type=F32)

    qs = pl.BlockSpec((tq, HEAD_X), lambda h, i: (i, h))
    ms = pl.BlockSpec((MEM_LEN, HEAD_X), lambda h, i: (0, h))
    return pl.pallas_call(
        body, name=name, grid=(N_HEADS_X, t // tq),
        in_specs=[qs, ms, pl.BlockSpec((MEM_LEN, HEAD_X), lambda h, i: (0, h + N_HEADS_X)), qs],
        out_specs=[qs, ms, ms],
        out_shape=[jax.ShapeDtypeStruct((t, D_MODEL), BF16), jax.ShapeDtypeStruct((MEM_LEN, D_MODEL), F32),
                   jax.ShapeDtypeStruct((MEM_LEN, D_MODEL), F32)],
        compiler_params=_params(("parallel", "arbitrary")))(q, kv, kv, do)


def _pool_counts(t, win):
    pos = lax.broadcasted_iota(jnp.int32, (t, 1), 0).astype(F32) + 1.0
    return 1.0 / jnp.minimum(pos, float(win))


def _pool_delta(xv, win):
    s, step = xv, 1
    while step < win:
        s = s + _shift_down(s, step)
        step *= 2
    return s * _pool_counts(xv.shape[0], win) - xv


def _pool_delta_t(dv, win):
    s, step = dv * _pool_counts(dv.shape[0], win), 1
    while step < win:
        s = s + _shift_up(s, step)
        step *= 2
    return s - dv


def _pool_fwd(xn, w, scale, res, name):
    t = xn.shape[0]

    def make_branch(win, xn_ref, w_ref, s_ref, r_ref, o_ref):
        def branch():
            dl = _pool_delta(xn_ref[...], win)
            y = jnp.dot(dl.astype(BF16), w_ref[0], preferred_element_type=F32)
            o_ref[...] = r_ref[...] + y * s_ref[...]
        return branch

    def body(xn_ref, w_ref, s_ref, r_ref, o_ref):
        for gi, win in enumerate(POOL_WINDOWS):
            pl.when(pl.program_id(0) == gi)(make_branch(win, xn_ref, w_ref, s_ref, r_ref, o_ref))

    blk = pl.BlockSpec((t, POOL_GROUP), lambda g: (0, g))
    return pl.pallas_call(
        body, name=name, grid=(len(POOL_WINDOWS),),
        in_specs=[blk, pl.BlockSpec((1, POOL_GROUP, POOL_GROUP), lambda g: (g, 0, 0)),
                  pl.BlockSpec((1, POOL_GROUP), lambda g: (0, g)), blk],
        out_specs=blk, out_shape=jax.ShapeDtypeStruct((t, D_MODEL), F32),
        compiler_params=_params(("parallel",)))(xn, w, scale, res)


def _pool_bwd(xn, w, scale, dmix, name):
    t = xn.shape[0]

    def make_branch(win, xn_ref, w_ref, s_ref, d_ref, dxn_ref, dw_ref, ds_ref):
        def branch():
            dl = _pool_delta(xn_ref[...], win).astype(BF16)
            wv = w_ref[0]
            dm = d_ref[...]
            y = jnp.dot(dl, wv, preferred_element_type=F32)
            ds_ref[...] = jnp.sum(dm * y, axis=0, keepdims=True)
            dy = (dm * s_ref[...]).astype(BF16)
            dw_ref[0] = lax.dot_general(dl, dy, (((0,), (0,)), ((), ())), preferred_element_type=F32)
            ddl = lax.dot_general(dy, wv, (((1,), (1,)), ((), ())), preferred_element_type=F32)
            dxn_ref[...] = _pool_delta_t(ddl, win)
        return branch

    def body(*refs):
        for gi, win in enumerate(POOL_WINDOWS):
            pl.when(pl.program_id(0) == gi)(make_branch(win, *refs))

    blk = pl.BlockSpec((t, POOL_GROUP), lambda g: (0, g))
    wspec = pl.BlockSpec((1, POOL_GROUP, POOL_GROUP), lambda g: (g, 0, 0))
    vec = pl.BlockSpec((1, POOL_GROUP), lambda g: (0, g))
    return pl.pallas_call(
        body, name=name, grid=(len(POOL_WINDOWS),), in_specs=[blk, wspec, vec, blk], out_specs=[blk, wspec, vec],
        out_shape=[jax.ShapeDtypeStruct((t, D_MODEL), F32),
                   jax.ShapeDtypeStruct((len(POOL_WINDOWS), POOL_GROUP, POOL_GROUP), F32),
                   jax.ShapeDtypeStruct((1, D_MODEL), F32)],
        compiler_params=_params(("parallel",)))(xn, w, scale, dmix)


def _qkv_conv(h, wr):
    return (wr[3:4, :] * h + wr[2:3, :] * _shift_down(h, 1) + wr[1:2, :] * _shift_down(h, 2)
            + wr[0:1, :] * _shift_down(h, 3))


def _qkv_pre_fwd(h, w, col0, ncols, normalize, scale, name):
    t = h.shape[0]

    def body(h_ref, w_ref, o_ref):
        c = _qkv_conv(h_ref[...], w_ref[...])
        s = c * _sigmoid(c)
        if normalize:
            s = s * lax.rsqrt(jnp.sum(s * s, axis=-1, keepdims=True) + 1e-6) * scale
        o_ref[...] = s

    return pl.pallas_call(
        body, name=name, grid=(ncols,),
        in_specs=[pl.BlockSpec((t, HEAD_A), lambda j: (0, j + col0)), pl.BlockSpec((CONV_A, HEAD_A), lambda j: (0, j + col0))],
        out_specs=pl.BlockSpec((t, HEAD_A), lambda j: (0, j)), out_shape=jax.ShapeDtypeStruct((t, ncols * HEAD_A), F32),
        compiler_params=_params(("parallel",)))(h, w)


def _qkv_pre_bwd(h, w, dy, col0, ncols, normalize, scale, name):
    t = h.shape[0]

    def body(h_ref, w_ref, dy_ref, dh_ref, dw_ref):
        hv, wr, dyv = h_ref[...], w_ref[...], dy_ref[...]
        h1, h2, h3 = _shift_down(hv, 1), _shift_down(hv, 2), _shift_down(hv, 3)
        c = wr[3:4, :] * hv + wr[2:3, :] * h1 + wr[1:2, :] * h2 + wr[0:1, :] * h3
        s, dsilu = _silu_and_grad(c)
        if normalize:
            r = lax.rsqrt(jnp.sum(s * s, axis=-1, keepdims=True) + 1e-6)
            y = s * r
            dyv = dyv * scale
            ds = r * (dyv - y * jnp.sum(dyv * y, axis=-1, keepdims=True))
        else:
            ds = dyv
        dc = ds * dsilu
        dh = (wr[3:4, :] * dc + wr[2:3, :] * _shift_up(dc, 1) + wr[1:2, :] * _shift_up(dc, 2)
              + wr[0:1, :] * _shift_up(dc, 3))
        dh_ref[...] = dh.astype(BF16)
        dw_ref[0:1, :] = jnp.sum(dc * h3, axis=0, keepdims=True)
        dw_ref[1:2, :] = jnp.sum(dc * h2, axis=0, keepdims=True)
        dw_ref[2:3, :] = jnp.sum(dc * h1, axis=0, keepdims=True)
        dw_ref[3:4, :] = jnp.sum(dc * hv, axis=0, keepdims=True)

    return pl.pallas_call(
        body, name=name, grid=(ncols,),
        in_specs=[pl.BlockSpec((t, HEAD_A), lambda j: (0, j + col0)), pl.BlockSpec((CONV_A, HEAD_A), lambda j: (0, j + col0)),
                  pl.BlockSpec((t, HEAD_A), lambda j: (0, j))],
        out_specs=[pl.BlockSpec((t, HEAD_A), lambda j: (0, j)), pl.BlockSpec((CONV_A, HEAD_A), lambda j: (0, j))],
        out_shape=[jax.ShapeDtypeStruct((t, ncols * HEAD_A), BF16), jax.ShapeDtypeStruct((CONV_A, ncols * HEAD_A), F32)],
        compiler_params=_params(("parallel",)))(h, w, dy)


def _softplus(x):
    return jnp.maximum(x, 0.0) + jnp.log1p(jnp.exp(-jnp.abs(x)))


def _gates_fwd(ba, arow, brow, name):
    t = ba.shape[0]

    def body(x_ref, a_ref, b_ref, o_ref):
        xv = x_ref[...]
        lane = lax.broadcasted_iota(jnp.int32, xv.shape, 1)
        beta = _sigmoid(xv)
        g = -jnp.exp(a_ref[...]) * _softplus(xv + b_ref[...])
        o_ref[...] = jnp.where(lane < N_HEADS_A, beta, jnp.where(lane < 2 * N_HEADS_A, g, 0.0))

    return pl.pallas_call(body, name=name, out_shape=jax.ShapeDtypeStruct((t, LANE), F32),
                          compiler_params=_params())(ba, arow, brow)


def _gates_bwd(ba, arow, brow, dgb, name):
    t = ba.shape[0]

    def body(x_ref, a_ref, b_ref, d_ref, dx_ref, da_ref, db_ref):
        xv, dv = x_ref[...], d_ref[...]
        lane = lax.broadcasted_iota(jnp.int32, xv.shape, 1)
        beta = _sigmoid(xv)
        ea = jnp.exp(a_ref[...])
        z = xv + b_ref[...]
        dgv = jnp.where((lane >= N_HEADS_A) & (lane < 2 * N_HEADS_A), dv, 0.0) * (-ea)
        dz = dgv * _sigmoid(z)
        dx = jnp.where(lane < N_HEADS_A, dv * beta * (1.0 - beta), dz)
        dx_ref[...] = dx.astype(BF16)
        db_ref[...] = jnp.sum(dz, axis=0, keepdims=True)
        da_ref[...] = jnp.sum(dgv * _softplus(z), axis=0, keepdims=True)

    return pl.pallas_call(
        body, name=name,
        out_shape=[jax.ShapeDtypeStruct((t, LANE), BF16), jax.ShapeDtypeStruct((1, LANE), F32),
                   jax.ShapeDtypeStruct((1, LANE), F32)],
        compiler_params=_params())(ba, arow, brow, dgb)


def _dot(a, b, prec=None):
    if prec is None:
        return jnp.dot(a.astype(BF16), b.astype(BF16), preferred_element_type=F32)
    return jnp.dot(a, b, precision=prec, preferred_element_type=F32)


def _dot_nt(a, b, prec=None):
    if prec is None:
        a, b = a.astype(BF16), b.astype(BF16)
    return lax.dot_general(a, b, (((1,), (1,)), ((), ())), precision=prec, preferred_element_type=F32)


def _dot_tn(a, b, prec=None):
    if prec is None:
        a, b = a.astype(BF16), b.astype(BF16)
    return lax.dot_general(a, b, (((0,), (0,)), ((), ())), precision=prec, preferred_element_type=F32)


def _gdr_chunk_terms(k, beta, g):
    c = GDR_CHUNK
    row = lax.broadcasted_iota(jnp.int32, (c, c), 0)
    col = lax.broadcasted_iota(jnp.int32, (c, c), 1)
    causal, strict = row >= col, row > col
    gcum = _dot(causal.astype(F32), jnp.broadcast_to(g, (c, c)), HIGHEST)
    diff = gcum - gcum.T
    decay = jnp.where(causal, jnp.exp(jnp.where(causal, diff, 0.0)), 0.0)
    kb = k * beta
    kk = _dot_nt(kb, k)
    return row, col, causal, strict, gcum, decay, kb, kk


def _unit_lower_inverse(a):
    c = a.shape[0]
    eye = (lax.broadcasted_iota(jnp.int32, (c, c), 0) == lax.broadcasted_iota(jnp.int32, (c, c), 1)).astype(F32)
    p = -a
    inv = eye + p
    step = 1
    while 2 * step < c:
        p = _dot(p, p, HIGH)
        inv = inv + _dot(inv, p, HIGH)
        step *= 2
    return inv


def _gdr_fwd(q, k, v, gb, name):
    t = q.shape[0]
    c = GDR_CHUNK
    n = t // c

    def body(q_ref, k_ref, v_ref, gb_ref, o_ref, tm_ref, s_ref, state):
        @pl.when(pl.program_id(1) == 0)
        def _():
            state[...] = jnp.zeros_like(state)

        qv, kv, vv = q_ref[...], k_ref[...], v_ref[...]
        beta, g = gb_ref[0, :, 0:1], gb_ref[0, :, 1:2]
        row, col, causal, strict, gcum, decay, kb, kk = _gdr_chunk_terms(kv, beta, g)
        tm = _unit_lower_inverse(jnp.where(strict, kk * decay, 0.0))
        e = jnp.exp(gcum)
        u = _dot(tm, vv * beta, HIGH)
        w = _dot(tm, kb * e, HIGH)
        p = jnp.where(causal, _dot_nt(qv, kv) * decay, 0.0)
        s = state[...]
        s_ref[0, 0] = s
        tm_ref[0, 0] = tm
        vn = u - _dot(w, s)
        o_ref[...] = _dot(qv * e, s) + _dot(p, vn)
        glast = gcum[c - 1:c, :]
        state[...] = s * jnp.exp(glast) + _dot_tn(kv * jnp.exp(glast - gcum), vn)

    blk = pl.BlockSpec((c, HEAD_A), lambda h, i: (i, h))
    mat = pl.BlockSpec((1, 1, c, c), lambda h, i: (h, i, 0, 0))
    return pl.pallas_call(
        body, name=name, grid=(N_HEADS_A, n),
        in_specs=[blk, blk, blk, pl.BlockSpec((1, c, 2), lambda h, i: (h, i, 0))],
        out_specs=[blk, mat, mat],
        out_shape=[jax.ShapeDtypeStruct((t, WIDTH_A), F32), jax.ShapeDtypeStruct((N_HEADS_A, n, c, c), F32),
                   jax.ShapeDtypeStruct((N_HEADS_A, n, HEAD_A, HEAD_A), F32)],
        scratch_shapes=[pltpu.VMEM((HEAD_A, HEAD_A), F32)],
        compiler_params=_params(("parallel", "arbitrary")))(q, k, v, gb)


def _gdr_bwd(q, k, v, gb, tm_all, s_all, do, name):
    t = q.shape[0]
    c = GDR_CHUNK
    n = t // c

    def body(q_ref, k_ref, v_ref, gb_ref, tm_ref, s_ref, do_ref, dq_ref, dk_ref, dv_ref, dgb_ref, dstate):
        @pl.when(pl.program_id(1) == 0)
        def _():
            dstate[...] = jnp.zeros_like(dstate)

        qv, kv, vv, dov = q_ref[...], k_ref[...], v_ref[...], do_ref[...]
        beta, g = gb_ref[0, :, 0:1], gb_ref[0, :, 1:2]
        tm, s, dsp = tm_ref[0, 0], s_ref[0, 0], dstate[...]
        row, col, causal, strict, gcum, decay, kb, kk = _gdr_chunk_terms(kv, beta, g)
        e = jnp.exp(gcum)
        vb, kbe = vv * beta, kb * e
        u = _dot(tm, vb, HIGH)
        w = _dot(tm, kbe, HIGH)
        qk = _dot_nt(qv, kv)
        p = jnp.where(causal, qk * decay, 0.0)
        vn = u - _dot(w, s)
        glast = gcum[c - 1:c, :]
        el = jnp.exp(glast)
        f = jnp.exp(glast - gcum)
        kd = kv * f
        qe = qv * e

        dvn = _dot_tn(p, dov) + _dot(kd, dsp)
        dglast = el[:, 0:1] * jnp.sum(s * dsp, keepdims=True)
        dkd = _dot_nt(vn, dsp)
        dk = dkd * f
        df = jnp.sum(dkd * kv, axis=1, keepdims=True) * f[:, 0:1]
        dglast = dglast + jnp.sum(df, keepdims=True)
        dgc = -df
        dp = jnp.where(causal, _dot_nt(dov, vn), 0.0)
        dqe = _dot_nt(dov, s)
        dq = dqe * e
        de = jnp.sum(dqe * qv, axis=1, keepdims=True)
        dstate[...] = dsp * el + _dot_tn(qe, dov) - _dot_tn(w, dvn)
        dw = -_dot_nt(dvn, s)
        dvb = _dot_tn(tm, dvn, HIGH)
        dkbe = _dot_tn(tm, dw, HIGH)
        da = -jnp.where(strict, _dot_nt(dvb, u) + _dot_nt(dkbe, w), 0.0)
        dkk = da * decay
        dqk = dp * decay
        dd = da * kk + dp * qk
        dq = dq + _dot(dqk, kv)
        dk = dk + _dot_tn(dqk, qv)
        dkb = _dot(dkk, kv) + dkbe * e
        dk = dk + _dot_tn(dkk, kb)
        de = de + jnp.sum(dkbe * kb, axis=1, keepdims=True)
        dk = dk + dkb * beta
        dbeta = jnp.sum(dkb * kv, axis=1, keepdims=True) + jnp.sum(dvb * vv, axis=1, keepdims=True)
        m = dd * decay
        dgc = dgc + jnp.sum(m, axis=1, keepdims=True) - jnp.sum(m.T, axis=1, keepdims=True)
        dgc = dgc + de * e[:, 0:1]
        dgc = dgc + jnp.where(row[:, 0:1] == c - 1, dglast, 0.0)
        dg = _dot((row <= col).astype(F32), jnp.broadcast_to(dgc, (c, c)), HIGHEST)
        dq_ref[...] = dq
        dk_ref[...] = dk
        dv_ref[...] = dvb * beta
        dgb_ref[0, :, 0:1] = dbeta
        dgb_ref[0, :, 1:2] = dg[:, 0:1]

    blk = pl.BlockSpec((c, HEAD_A), lambda h, i: (n - 1 - i, h))
    mat = pl.BlockSpec((1, 1, c, c), lambda h, i: (h, n - 1 - i, 0, 0))
    gsp = pl.BlockSpec((1, c, 2), lambda h, i: (h, n - 1 - i, 0))
    return pl.pallas_call(
        body, name=name, grid=(N_HEADS_A, n),
        in_specs=[blk, blk, blk, gsp, mat, mat, blk], out_specs=[blk, blk, blk, gsp],
        out_shape=[jax.ShapeDtypeStruct((t, WIDTH_A), F32)] * 3 + [jax.ShapeDtypeStruct((N_HEADS_A, t, 2), F32)],
        scratch_shapes=[pltpu.VMEM((HEAD_A, HEAD_A), F32)],
        compiler_params=_params(("parallel", "arbitrary")))(q, k, v, gb, tm_all, s_all, do)


def _onorm_fwd(o, gate, g, name):
    t = o.shape[0]

    def body(o_ref, gate_ref, g_ref, y_ref):
        ov, gv = o_ref[...], gate_ref[...]
        r = lax.rsqrt(jnp.mean(ov * ov, axis=-1, keepdims=True) + RMS_EPS)
        y_ref[...] = (ov * r * g_ref[...] * gv * _sigmoid(gv)).astype(BF16)

    blk = pl.BlockSpec((t, HEAD_A), lambda j: (0, j))
    return pl.pallas_call(
        body, name=name, grid=(N_HEADS_A,), in_specs=[blk, blk, pl.BlockSpec((1, HEAD_A), lambda j: (0, 0))],
        out_specs=blk, out_shape=jax.ShapeDtypeStruct((t, WIDTH_A), BF16),
        compiler_params=_params(("parallel",)))(o, gate, g)


def _onorm_bwd(o, gate, g, dy, name):
    t = o.shape[0]

    def body(o_ref, gate_ref, g_ref, dy_ref, do_ref, dgate_ref, dg_ref):
        @pl.when(pl.program_id(0) == 0)
        def _():
            dg_ref[...] = jnp.zeros_like(dg_ref)

        ov, gv, dyv = o_ref[...], gate_ref[...], dy_ref[...].astype(F32)
        r = lax.rsqrt(jnp.mean(ov * ov, axis=-1, keepdims=True) + RMS_EPS)
        oh = ov * r
        sg, dsg = _silu_and_grad(gv)
        dgate_ref[...] = (dyv * oh * g_ref[...] * dsg).astype(BF16)
        dn = dyv * sg
        dg_ref[...] += jnp.sum(dn * oh, axis=0, keepdims=True)
        dng = dn * g_ref[...]
        do_ref[...] = r * (dng - oh * jnp.mean(dng * oh, axis=-1, keepdims=True))

    blk = pl.BlockSpec((t, HEAD_A), lambda j: (0, j))
    vec = pl.BlockSpec((1, HEAD_A), lambda j: (0, 0))
    return pl.pallas_call(
        body, name=name, grid=(N_HEADS_A,), in_specs=[blk, blk, vec, blk], out_specs=[blk, blk, vec],
        out_shape=[jax.ShapeDtypeStruct((t, WIDTH_A), F32), jax.ShapeDtypeStruct((t, WIDTH_A), BF16),
                   jax.ShapeDtypeStruct((1, HEAD_A), F32)],
        compiler_params=_params(("arbitrary",)))(o, gate, g, dy)


def _cmul(ar, ai, br, bi):
    return ar * br - ai * bi, ar * bi + ai * br


def _scan_tables(ar, ai, reverse):
    p1 = (ar, ai)
    p2 = _cmul(*p1, *p1)
    p4 = _cmul(*p2, *p2)
    p8 = _cmul(*p4, *p4)
    p3 = _cmul(*p2, *p1)
    p5 = _cmul(*p4, *p1)
    p6 = _cmul(*p4, *p2)
    p7 = _cmul(*p4, *p3)
    pows = [p1, p2, p3, p4, p5, p6, p7, p8]
    rows = lax.broadcasted_iota(jnp.int32, (8, ar.shape[1]), 0)
    tr = jnp.zeros((8, ar.shape[1]), F32)
    ti = jnp.zeros((8, ar.shape[1]), F32)
    for r in range(8):
        pw = pows[7 - r] if reverse else pows[r]
        tr = jnp.where(rows == r, pw[0], tr)
        ti = jnp.where(rows == r, pw[1], ti)
    return p1, p2, p4, p8, tr, ti


def _tile_scan(xr, xi, p1, p2, p4, reverse):
    rows = lax.broadcasted_iota(jnp.int32, xr.shape, 0)
    for s, (pr, pi) in ((1, p1), (2, p2), (4, p4)):
        if reverse:
            keep = rows < 8 - s
            sr, si = pltpu.roll(xr, 8 - s, 0), pltpu.roll(xi, 8 - s, 0)
        else:
            keep = rows >= s
            sr, si = pltpu.roll(xr, s, 0), pltpu.roll(xi, s, 0)
        sr, si = jnp.where(keep, sr, 0.0), jnp.where(keep, si, 0.0)
        mr, mi = _cmul(pr, pi, sr, si)
        xr, xi = xr + mr, xi + mi
    return xr, xi


def _s5_scan_fwd(bu, a, name, tb=512):
    t = bu.shape[0]
    cb = SCAN_CB
    nt = t // tb

    def body(b_ref, a_ref, x_ref, carry):
        @pl.when(pl.program_id(1) == 0)
        def _():
            carry[...] = jnp.zeros_like(carry)

        ar, ai = a_ref[:, 0:cb], a_ref[:, cb:2 * cb]
        p1, p2, p4, p8, tr, ti = _scan_tables(ar, ai, False)

        def step(j, c):
            cr, ci = c
            i = pl.multiple_of(j * 8, 8)
            xr, xi = _tile_scan(b_ref[pl.ds(i, 8), 0:cb], b_ref[pl.ds(i, 8), cb:2 * cb], p1, p2, p4, False)
            mr, mi = _cmul(tr, ti, cr, ci)
            xr, xi = xr + mr, xi + mi
            x_ref[pl.ds(i, 8), 0:cb] = xr
            x_ref[pl.ds(i, 8), cb:2 * cb] = xi
            return xr[7:8, :], xi[7:8, :]

        cr, ci = lax.fori_loop(0, tb // 8, step, (carry[0:1, :], carry[1:2, :]), unroll=2)
        carry[0:1, :] = cr
        carry[1:2, :] = ci

    blk = pl.BlockSpec((tb, 2 * cb), lambda j, i: (i, j))
    return pl.pallas_call(
        body, name=name, grid=(SSM_CH // cb, nt),
        in_specs=[blk, pl.BlockSpec((1, 2 * cb), lambda j, i: (0, j))], out_specs=blk,
        out_shape=jax.ShapeDtypeStruct((t, 2 * SSM_CH), F32), scratch_shapes=[pltpu.VMEM((8, cb), F32)],
        compiler_params=_params(("parallel", "arbitrary")))(bu, a)


def _s5_scan_bwd(dx, x, a, name, tb=512):
    t = dx.shape[0]
    cb = SCAN_CB
    nt = t // tb
    nj = tb // 8

    def body(d_ref, x_ref, xp_ref, a_ref, l_ref, da_ref, carry, acc):
        tblk = pl.program_id(1)

        @pl.when(tblk == 0)
        def _():
            carry[...] = jnp.zeros_like(carry)
            acc[...] = jnp.zeros_like(acc)

        ar, ai = a_ref[:, 0:cb], a_ref[:, cb:2 * cb]
        p1, p2, p4, p8, tr, ti = _scan_tables(ar, -ai, True)
        rows = lax.broadcasted_iota(jnp.int32, (8, cb), 0)

        def step(jj, c):
            cr, ci, sr_acc, si_acc = c
            j = nj - 1 - jj
            i = pl.multiple_of(j * 8, 8)
            lr, li = _tile_scan(d_ref[pl.ds(i, 8), 0:cb], d_ref[pl.ds(i, 8), cb:2 * cb], p1, p2, p4, True)
            mr, mi = _cmul(tr, ti, cr, ci)
            lr, li = lr + mr, li + mi
            l_ref[pl.ds(i, 8), 0:cb] = lr
            l_ref[pl.ds(i, 8), cb:2 * cb] = li
            ip = pl.multiple_of(jnp.maximum(j - 1, 0) * 8, 8)
            prev_r = jnp.where(j > 0, x_ref[pl.ds(ip, 8), 0:cb], xp_ref[:, 0:cb])
            prev_i = jnp.where(j > 0, x_ref[pl.ds(ip, 8), cb:2 * cb], xp_ref[:, cb:2 * cb])
            edge = jnp.where(jnp.logical_and(j == 0, tblk == nt - 1), 0.0, 1.0)
            xs_r = jnp.where(rows == 0, pltpu.roll(prev_r, 1, 0) * edge, pltpu.roll(x_ref[pl.ds(i, 8), 0:cb], 1, 0))
            xs_i = jnp.where(rows == 0, pltpu.roll(prev_i, 1, 0) * edge, pltpu.roll(x_ref[pl.ds(i, 8), cb:2 * cb], 1, 0))
            sr_acc = sr_acc + lr * xs_r + li * xs_i
            si_acc = si_acc + li * xs_r - lr * xs_i
            return lr[0:1, :], li[0:1, :], sr_acc, si_acc

        cr, ci, sr_acc, si_acc = lax.fori_loop(
            0, nj, step, (carry[0:1, :], carry[1:2, :], acc[:, 0:cb], acc[:, cb:2 * cb]))
        carry[0:1, :] = cr
        carry[1:2, :] = ci
        acc[:, 0:cb] = sr_acc
        acc[:, cb:2 * cb] = si_acc

        @pl.when(tblk == nt - 1)
        def _():
            da_ref[...] = jnp.sum(acc[...], axis=0, keepdims=True)

    blk = pl.BlockSpec((tb, 2 * cb), lambda j, i: (nt - 1 - i, j))
    prev = pl.BlockSpec((8, 2 * cb), lambda j, i: (jnp.maximum((nt - 1 - i) * (tb // 8) - 1, 0), j))
    vec = pl.BlockSpec((1, 2 * cb), lambda j, i: (0, j))
    return pl.pallas_call(
        body, name=name, grid=(SSM_CH // cb, nt), in_specs=[blk, blk, prev, vec], out_specs=[blk, vec],
        out_shape=[jax.ShapeDtypeStruct((t, 2 * SSM_CH), F32), jax.ShapeDtypeStruct((1, 2 * SSM_CH), F32)],
        scratch_shapes=[pltpu.VMEM((8, cb), F32), pltpu.VMEM((8, 2 * cb), F32)],
        compiler_params=_params(("parallel", "arbitrary")))(dx, x, x, a)


def _glu_fwd(yc, u, dvec, wg, bg, name, tr=256):
    t = yc.shape[0]

    def body(yc_ref, u_ref, d_ref, w_ref, b_ref, yl_ref, yb_ref):
        yl = yc_ref[...] + d_ref[...] * u_ref[...]
        yl_ref[...] = yl
        yg, _ = _gelu_and_grad(yl)
        z = jnp.dot(yg.astype(BF16), w_ref[...], preferred_element_type=F32) + b_ref[...]
        yb_ref[...] = (yg * _sigmoid(z)).astype(BF16)

    blk = pl.BlockSpec((tr, SSM_WIDTH), lambda i: (i, 0))
    vec = pl.BlockSpec((1, SSM_WIDTH), lambda i: (0, 0))
    return pl.pallas_call(
        body, name=name, grid=(t // tr,),
        in_specs=[blk, blk, vec, pl.BlockSpec((SSM_WIDTH, SSM_WIDTH), lambda i: (0, 0)), vec],
        out_specs=[blk, blk],
        out_shape=[jax.ShapeDtypeStruct((t, SSM_WIDTH), F32), jax.ShapeDtypeStruct((t, SSM_WIDTH), BF16)],
        compiler_params=_params(("parallel",)))(yc, u, dvec, wg, bg)


def _glu_bwd(yl, u, dvec, wg, bg, dyb, name, tr=256):
    t = yl.shape[0]

    def body(yl_ref, u_ref, d_ref, w_ref, b_ref, dy_ref, dyl_ref, du_ref, dw_ref, db_ref, dd_ref):
        @pl.when(pl.program_id(0) == 0)
        def _():
            dw_ref[...] = jnp.zeros_like(dw_ref)
            db_ref[...] = jnp.zeros_like(db_ref)
            dd_ref[...] = jnp.zeros_like(dd_ref)

        ylv, dyv, wv = yl_ref[...], dy_ref[...].astype(F32), w_ref[...]
        yg, dgelu = _gelu_and_grad(ylv)
        ygb = yg.astype(BF16)
        z = jnp.dot(ygb, wv, preferred_element_type=F32) + b_ref[...]
        sg = _sigmoid(z)
        dz = dyv * yg * sg * (1.0 - sg)
        dzb = dz.astype(BF16)
        dyg = dyv * sg + lax.dot_general(dzb, wv, (((1,), (1,)), ((), ())), preferred_element_type=F32)
        dyl = dyg * dgelu
        dyl_ref[...] = dyl.astype(BF16)
        du_ref[...] = dyl * d_ref[...]
        dw_ref[...] += lax.dot_general(ygb, dzb, (((0,), (0,)), ((), ())), preferred_element_type=F32)
        db_ref[...] += jnp.sum(dz, axis=0, keepdims=True)
        dd_ref[...] += jnp.sum(dyl * u_ref[...], axis=0, keepdims=True)

    blk = pl.BlockSpec((tr, SSM_WIDTH), lambda i: (i, 0))
    vec = pl.BlockSpec((1, SSM_WIDTH), lambda i: (0, 0))
    wsp = pl.BlockSpec((SSM_WIDTH, SSM_WIDTH), lambda i: (0, 0))
    return pl.pallas_call(
        body, name=name, grid=(t // tr,), in_specs=[blk, blk, vec, wsp, vec, blk],
        out_specs=[blk, blk, wsp, vec, vec],
        out_shape=[jax.ShapeDtypeStruct((t, SSM_WIDTH), BF16), jax.ShapeDtypeStruct((t, SSM_WIDTH), F32),
                   jax.ShapeDtypeStruct((SSM_WIDTH, SSM_WIDTH), F32), jax.ShapeDtypeStruct((1, SSM_WIDTH), F32),
                   jax.ShapeDtypeStruct((1, SSM_WIDTH), F32)],
        compiler_params=_params(("arbitrary",)))(yl, u, dvec, wg, bg, dyb)


def _mesh_pos():
    return lax.axis_index("x"), lax.axis_index("y"), lax.axis_index("c")


def _all_gather(arrays, name):
    na = len(arrays)

    def body(*refs):
        ins, outs = refs[:na], refs[na:2 * na]
        send_sems, recv_sems, local_sems = refs[2 * na:]
        x, y, c = _mesh_pos()
        me, sibling = (x, y, c), (x, y, 1 - c)
        chips = [(1 - x, y), (x, 1 - y), (1 - x, 1 - y)]
        waits = []
        for ai in range(na):
            in_ref, out_ref = ins[ai], outs[ai]

            def slot(px, py, pc, out_ref=out_ref):
                return out_ref.at[4 * px + 2 * py + pc]

            def copy(kk, block, to, src=None, ai=ai, slot=slot):
                return pltpu.make_async_remote_copy(
                    src_ref=slot(*block) if src is None else src, dst_ref=slot(*block),
                    send_sem=send_sems.at[ai, kk], recv_sem=recv_sems.at[ai, kk], device_id=to, device_id_type=MESH)

            mine = pltpu.make_async_copy(in_ref, slot(*me), local_sems.at[ai])
            mine.start()
            first = [copy(0, me, sibling, src=in_ref)]
            first += [copy(1 + j, me, (*chip, c), src=in_ref) for j, chip in enumerate(chips)]
            for cp in first:
                cp.start()
            waits.append((copy, mine, first))
        sends = []
        for ai in range(na):
            copy, mine, first = waits[ai]
            passed = [copy(4 + j, (*chip, c), sibling) for j, chip in enumerate(chips)]
            for j, chip in enumerate(chips):
                copy(1 + j, (*chip, c), me).wait_recv()
                passed[j].start()
            sends.append(passed)
        for ai in range(na):
            copy, mine, first = waits[ai]
            copy(0, sibling, me).wait_recv()
            for j, chip in enumerate(chips):
                copy(4 + j, (*chip, 1 - c), me).wait_recv()
            for cp in first + sends[ai]:
                cp.wait_send()
            mine.wait()

    any_spec = pl.BlockSpec(memory_space=pl.ANY)
    return pl.pallas_call(
        body, name=name, in_specs=[any_spec] * na, out_specs=[any_spec] * na,
        out_shape=[jax.ShapeDtypeStruct((N_DEV,) + a.shape, a.dtype) for a in arrays],
        scratch_shapes=[pltpu.SemaphoreType.DMA((na, 7)), pltpu.SemaphoreType.DMA((na, 7)),
                        pltpu.SemaphoreType.DMA((na,))],
        compiler_params=pltpu.CompilerParams(has_side_effects=True))(*arrays)


def _swap_sibling(arrays, name):
    na = len(arrays)
    offs = np.concatenate([[0], np.cumsum([a.shape[1] for a in arrays])]).astype(int)
    rows = int(offs[-1])

    def body(*refs):
        ins, recv_ref = refs[:na], refs[na]
        send_sems, recv_sems = refs[na + 1:]
        x, y, c = _mesh_pos()
        started = []
        for ai in range(na):
            span = pl.ds(int(offs[ai]), arrays[ai].shape[1])
            for k in range(4):
                remote = pltpu.make_async_remote_copy(
                    src_ref=ins[ai].at[2 * k + 1 - c], dst_ref=recv_ref.at[k, span], send_sem=send_sems.at[ai, k],
                    recv_sem=recv_sems.at[ai, k], device_id=(x, y, 1 - c), device_id_type=MESH)
                remote.start()
                started.append(remote)
        for remote in started:
            remote.wait()

    any_spec = pl.BlockSpec(memory_space=pl.ANY)
    return pl.pallas_call(
        body, name=name, in_specs=[any_spec] * na, out_specs=any_spec,
        out_shape=jax.ShapeDtypeStruct((4, rows, PACK_COLS), arrays[0].dtype),
        scratch_shapes=[pltpu.SemaphoreType.DMA((na, 4)), pltpu.SemaphoreType.DMA((na, 4))])(*arrays)


def _exchange_chips(send, name):
    def body(s_ref, o_ref, send_sems, recv_sems):
        x, y, c = _mesh_pos()
        chips = [(1 - x, y), (x, 1 - y), (1 - x, 1 - y)]
        cps = [pltpu.make_async_remote_copy(
            src_ref=s_ref.at[2 * cx + cy], dst_ref=o_ref.at[j], send_sem=send_sems.at[j], recv_sem=recv_sems.at[j],
            device_id=(cx, cy, c), device_id_type=MESH) for j, (cx, cy) in enumerate(chips)]
        for cp in cps:
            cp.start()
        for cp in cps:
            cp.wait()

    any_spec = pl.BlockSpec(memory_space=pl.ANY)
    return pl.pallas_call(
        body, name=name, in_specs=[any_spec], out_specs=any_spec,
        out_shape=jax.ShapeDtypeStruct((3,) + send.shape[1:], send.dtype),
        scratch_shapes=[pltpu.SemaphoreType.DMA((3,)), pltpu.SemaphoreType.DMA((3,))])(send)


def _pair_sum(keep, recv, name, tr=464):
    nchip, rows, cols = keep.shape

    def body(g_ref, r_ref, o_ref):
        o_ref[...] = (g_ref[...].astype(F32) + r_ref[...].astype(F32)).astype(BF16)

    blk = pl.BlockSpec((1, tr, cols), lambda k, i: (k, i, 0))
    return pl.pallas_call(
        body, name=name, grid=(nchip, rows // tr), in_specs=[blk, blk], out_specs=blk,
        out_shape=jax.ShapeDtypeStruct((nchip, rows, cols), BF16),
        compiler_params=_params(("parallel", "parallel")))(keep, recv)


def _chip_sum(own, others, name, tr=464):
    _, rows, cols = own.shape
    chip = (2 * lax.axis_index("x") + lax.axis_index("y")).astype(jnp.int32).reshape(1)

    def body(chip_ref, own_ref, oth_ref, o_ref):
        del chip_ref
        acc = own_ref[0].astype(F32)
        for j in range(3):
            acc = acc + oth_ref[j].astype(F32)
        o_ref[...] = acc

    grid_spec = pltpu.PrefetchScalarGridSpec(
        num_scalar_prefetch=1, grid=(rows // tr,),
        in_specs=[pl.BlockSpec((1, tr, cols), lambda i, chip_ref: (chip_ref[0], i, 0)),
                  pl.BlockSpec((3, tr, cols), lambda i, chip_ref: (0, i, 0))],
        out_specs=pl.BlockSpec((tr, cols), lambda i, chip_ref: (i, 0)))
    return pl.pallas_call(
        body, name=name, grid_spec=grid_spec, out_shape=jax.ShapeDtypeStruct((rows, cols), F32),
        compiler_params=_params(("parallel",)))(chip, own, others)


def _sum_leading(parts, name, tr=464):
    nparts, rows, cols = parts.shape
    tr = tr if rows % tr == 0 else rows

    def body(p_ref, o_ref):
        acc = p_ref[0].astype(F32)
        for i in range(1, nparts):
            acc = acc + p_ref[i].astype(F32)
        o_ref[...] = acc

    return pl.pallas_call(
        body, name=name, grid=(rows // tr,),
        in_specs=[pl.BlockSpec((nparts, tr, cols), lambda i: (0, i, 0))],
        out_specs=pl.BlockSpec((tr, cols), lambda i: (i, 0)), out_shape=jax.ShapeDtypeStruct((rows, cols), F32),
        compiler_params=_params(("parallel",)))(parts)


def _adamw(w, g, m, v, name):
    shape = w.shape
    cols = shape[-1]
    rows = int(np.prod(shape[:-1])) if len(shape) > 1 else 1
    w2, g2, m2, v2 = (a.reshape(rows, cols) for a in (w, g, m, v))
    tr = rows
    for cand in (512, 256, 128, 64, 32, 16, 8):
        if rows % cand == 0 and rows > cand:
            tr = cand
            break
    bc1, bc2 = 1.0 - ADAM_B1 ** ADAM_STEP, 1.0 - ADAM_B2 ** ADAM_STEP

    def body(w_ref, g_ref, m_ref, v_ref, d_ref, nm_ref, nv_ref):
        gv = g_ref[...]
        nm = ADAM_B1 * m_ref[...] + (1.0 - ADAM_B1) * gv
        nv = ADAM_B2 * v_ref[...] + (1.0 - ADAM_B2) * (gv * gv)
        nm_ref[...] = nm
        nv_ref[...] = nv
        d_ref[...] = -ADAM_LR * ((nm / bc1) / (jnp.sqrt(nv / bc2) + ADAM_EPS) + ADAM_WD * w_ref[...])

    blk = pl.BlockSpec((tr, cols), lambda i: (i, 0))
    outs = pl.pallas_call(
        body, name=name, grid=(rows // tr,), in_specs=[blk] * 4, out_specs=[blk] * 3,
        out_shape=[jax.ShapeDtypeStruct((rows, cols), F32)] * 3, compiler_params=_params(("parallel",)))(w2, g2, m2, v2)
    return tuple(o.reshape(shape) for o in outs)


WEIGHT_NAMES = ['norm_mix_g', 'norm_xa_g', 'norm_ffn_g', 'norm_mem_g', 'norm_final_g', 'w_in_ab', 'conv_qkv_a',
                'a_log_a', 'dt_bias_a', 'onorm_g_a', 'ssm_lambda_re', 'ssm_lambda_im', 'ssm_b_re', 'ssm_b_im',
                'ssm_c_re', 'ssm_c_im', 'ssm_d', 'ssm_log_dt', 'w_glu_b', 'b_glu_b', 'w_out_ab', 'pool_w',
                'pool_scale', 'xa_wq', 'xa_wkv', 'xa_wo', 'ffn_w_up', 'ffn_conv', 'ffn_w_down']
BIG_SHARDED = {'w_in_ab': ((1, 1024, 2568), 2), 'w_glu_b': ((1, 512, 512), 1), 'w_out_ab': ((1, 1024, 1024), 1),
               'pool_w': ((1, 4, 256, 256), 2), 'xa_wq': ((2, 1024, 1024), 1), 'xa_wkv': ((2, 1024, 2048), 2),
               'xa_wo': ((2, 1024, 1024), 1), 'ffn_w_up': ((2, 1024, 5632), 2), 'ffn_w_down': ((2, 2816, 1024), 1)}
SMALL_SHARDED = {'conv_qkv_a': ((1, 4, 1536), 2), 'pool_scale': ((1, 1024), 1), 'ffn_conv': ((2, 3, 5632), 2)}
REPLICATED = {'norm_mix_g': (2, 1024), 'norm_xa_g': (2, 1024), 'norm_ffn_g': (2, 1024), 'norm_mem_g': (1024,),
              'norm_final_g': (1024,), 'a_log_a': (1, 4), 'dt_bias_a': (1, 4), 'onorm_g_a': (1, 128),
              'ssm_lambda_re': (1, 32, 64), 'ssm_lambda_im': (1, 32, 64), 'ssm_b_re': (1, 32, 64, 16),
              'ssm_b_im': (1, 32, 64, 16), 'ssm_c_re': (1, 32, 16, 64), 'ssm_c_im': (1, 32, 16, 64),
              'ssm_d': (1, 32, 16), 'ssm_log_dt': (1, 32), 'b_glu_b': (1, 512)}
BIG_ROW_ALIGN = 464 * 8


def _shard_shape(shape, axis):
    return tuple(s // N_DEV if i == axis else s for i, s in enumerate(shape))


def _pad_rows(a, rows):
    return a if a.shape[0] == rows else jnp.concatenate([a, jnp.zeros((rows - a.shape[0],) + a.shape[1:], a.dtype)], 0)


def _round_up(n, m):
    return (n + m - 1) // m * m


def _pack_rows(flat_list, align):
    rows = []
    for a in flat_list:
        n = a.shape[0]
        r = _round_up(n, PACK_COLS) // PACK_COLS
        if n != r * PACK_COLS:
            a = jnp.concatenate([a, jnp.zeros((r * PACK_COLS - n,), a.dtype)])
        rows.append(a.reshape(r, PACK_COLS))
    out = jnp.concatenate(rows, 0)
    return _pad_rows(out, _round_up(out.shape[0], align))


def _row_offsets(sizes):
    offs, r = [], 0
    for n in sizes:
        offs.append(r)
        r += _round_up(n, PACK_COLS) // PACK_COLS
    return offs


def _split_shards(full, axis):
    shape = full.shape
    s = shape[axis] // N_DEV
    a = full.reshape(shape[:axis] + (N_DEV, s) + shape[axis + 1:])
    return jnp.moveaxis(a, axis, 0).reshape(N_DEV, -1)


def _merge_shards(pieces, shape, axis):
    sh = _shard_shape(shape, axis)
    a = pieces.reshape((N_DEV,) + sh)
    a = jnp.moveaxis(a, 0, axis)
    return a.reshape(shape)


_SCAN_NB = SSM_CH // SCAN_CB


def _to_scan_layout(m, axis):
    shape = m.shape
    m = m.reshape(shape[:axis] + (2, _SCAN_NB, SCAN_CB) + shape[axis + 1:])
    return jnp.swapaxes(m, axis, axis + 1).reshape(shape)


def _from_scan_layout(m, axis):
    shape = m.shape
    m = m.reshape(shape[:axis] + (_SCAN_NB, 2, SCAN_CB) + shape[axis + 1:])
    return jnp.swapaxes(m, axis, axis + 1).reshape(shape)


def _s5_discretise(lam_re, lam_im, b_re, b_im, log_dt):
    dt = jnp.exp(log_dt)[:, None]
    mag = jnp.exp(lam_re * dt)
    ang = lam_im * dt
    lb_re, lb_im = mag * jnp.cos(ang), mag * jnp.sin(ang)
    den = lam_re * lam_re + lam_im * lam_im
    nr, ni = lb_re - 1.0, lb_im
    coef_re = (nr * lam_re + ni * lam_im) / den
    coef_im = (ni * lam_re - nr * lam_im) / den
    bb_re = coef_re[..., None] * b_re - coef_im[..., None] * b_im
    bb_im = coef_re[..., None] * b_im + coef_im[..., None] * b_re
    return lb_re, lb_im, bb_re, bb_im


def _s5_matrices(lb_re, lb_im, bb_re, bb_im, c_re, c_im):
    eye = jnp.eye(N_GROUPS, dtype=F32)
    bmat = lambda bb: jnp.einsum('gph,gk->ghkp', bb, eye).reshape(SSM_WIDTH, SSM_CH)
    cmat = lambda cc: jnp.einsum('ghp,gk->gpkh', cc, eye).reshape(SSM_CH, SSM_WIDTH)
    b_in = _to_scan_layout(jnp.concatenate([bmat(bb_re), bmat(bb_im)], axis=1), 1)
    c_out = _to_scan_layout(jnp.concatenate([cmat(c_re), -cmat(c_im)], axis=0), 0)
    a_row = _to_scan_layout(jnp.concatenate([lb_re.reshape(1, SSM_CH), lb_im.reshape(1, SSM_CH)], axis=1), 1)
    return b_in, c_out, a_row


def _s5_matrix_grads(db_in, dc_out, da_row):
    db_nat = _from_scan_layout(db_in, 1)
    dc_nat = _from_scan_layout(dc_out, 0)
    da_nat = _from_scan_layout(da_row, 1)
    eye = jnp.eye(N_GROUPS, dtype=F32)
    bgrad = lambda m: jnp.einsum('ghkp,gk->gph', m.reshape(N_GROUPS, SSM_GROUP, N_GROUPS, SSM_STATE), eye)
    cgrad = lambda m: jnp.einsum('gpkh,gk->ghp', m.reshape(N_GROUPS, SSM_STATE, N_GROUPS, SSM_GROUP), eye)
    dbb_re, dbb_im = bgrad(db_nat[:, :SSM_CH]), bgrad(db_nat[:, SSM_CH:])
    dc_re, dc_im = cgrad(dc_nat[:SSM_CH]), -cgrad(dc_nat[SSM_CH:])
    dlb_re = da_nat[0, :SSM_CH].reshape(N_GROUPS, SSM_STATE)
    dlb_im = da_nat[0, SSM_CH:].reshape(N_GROUPS, SSM_STATE)
    return dlb_re, dlb_im, dbb_re, dbb_im, dc_re, dc_im


def _hybrid_fwd(xn, x, wts, p):
    sv = {}
    hq = _mm(xn, wts['w_qkv_t'], "nt", "l0_in_qkv")
    gate = _mm(xn, wts['w_gate_t'], "nt", "l0_in_gate")
    ba = _mm(xn, wts['w_ba_t'], "nt", "l0_in_ba")
    u = _mm(xn, wts['w_u_t'], "nt", "l0_in_u")
    conv = p['conv_qkv']
    q = _qkv_pre_fwd(hq, conv, 0, 4, True, HEAD_A ** -0.5, "l0_q_pre")
    k = _qkv_pre_fwd(hq, conv, 4, 4, True, 1.0, "l0_k_pre")
    v = _qkv_pre_fwd(hq, conv, 8, 4, False, 1.0, "l0_v_pre")
    gates = _gates_fwd(ba, p['arow'], p['brow'], "l0_gates")
    gb = jnp.stack([gates[:, 0:4].T, gates[:, 4:8].T], axis=-1)
    o, tm_all, s_all = _gdr_fwd(q, k, v, gb, "l0_gdr_fwd")
    y_a = _onorm_fwd(o, gate, p['onorm_g'], "l0_onorm")
    bu = _mm(u, p['b_in'], "nn", "l0_s5_bu")
    xs = _s5_scan_fwd(bu, p['a_row'], "l0_s5_scan")
    yc = _mm(xs, p['c_out'], "nn", "l0_s5_cx")
    yl, y_b = _glu_fwd(yc, u, p['d_row'], wts['w_glu'], p['b_glu'], "l0_glu")
    mixed = jnp.concatenate([y_a, y_b], axis=1)
    x1 = _mm(mixed, wts['w_out'], "nn", "l0_out", res=x)
    sv.update(hq=hq, gate=gate, ba=ba, u=u, q=q, k=k, v=v, gb=gb, o=o, tm=tm_all, s=s_all, xs=xs, yl=yl, mixed=mixed)
    return x1, sv


def _hybrid_bwd(dx1, xn, wts, p, sv):
    gr = {}
    dmixed = _mm(dx1, wts['w_out'], "nt", "l0_out_dx", out_dtype=BF16)
    gr['w_out_ab'] = _mm(sv['mixed'], dx1, "tn", "l0_out_dw", out_dtype=BF16)
    dya, dyb = dmixed[:, :WIDTH_A], dmixed[:, WIDTH_A:]
    dyl, du_direct, gr['w_glu_b'], gr['b_glu_b'], dd = _glu_bwd(
        sv['yl'], sv['u'], p['d_row'], wts['w_glu'], p['b_glu'], dyb, "l0_glu_bwd")
    dxs = _mm(dyl, p['c_out'], "nt", "l0_s5_cx_dx")
    dc_out = _mm(sv['xs'], dyl, "tn", "l0_s5_cx_dw")
    lam, da_row = _s5_scan_bwd(dxs, sv['xs'], p['a_row'], "l0_s5_scan_bwd")
    du = _mm(lam, p['b_in'], "nt", "l0_s5_bu_dx", res=du_direct, out_dtype=BF16)
    db_in = _mm(sv['u'], lam, "tn", "l0_s5_bu_dw")
    gr['s5'] = (db_in, dc_out, da_row, dd)
    do, dgate, gr['onorm_g_a'] = _onorm_bwd(sv['o'], sv['gate'], p['onorm_g'], dya, "l0_onorm_bwd")
    dq, dk, dv, dgb = _gdr_bwd(sv['q'], sv['k'], sv['v'], sv['gb'], sv['tm'], sv['s'], do, "l0_gdr_bwd")
    conv = p['conv_qkv']
    dhq_q, dcw_q = _qkv_pre_bwd(sv['hq'], conv, dq, 0, 4, True, HEAD_A ** -0.5, "l0_q_pre_bwd")
    dhq_k, dcw_k = _qkv_pre_bwd(sv['hq'], conv, dk, 4, 4, True, 1.0, "l0_k_pre_bwd")
    dhq_v, dcw_v = _qkv_pre_bwd(sv['hq'], conv, dv, 8, 4, False, 1.0, "l0_v_pre_bwd")
    gr['conv_qkv_a'] = jnp.concatenate([dcw_q, dcw_k, dcw_v], axis=1)
    dhq = jnp.concatenate([dhq_q, dhq_k, dhq_v], axis=1)
    dgates = jnp.concatenate([dgb[:, :, 0].T, dgb[:, :, 1].T, jnp.zeros((SEQ, LANE - 8), F32)], axis=1)
    dba, da_log, ddt_bias = _gates_bwd(sv['ba'], p['arow'], p['brow'], dgates, "l0_gates_bwd")
    gr['a_log_a'], gr['dt_bias_a'] = da_log[:, 4:8], ddt_bias[:, 4:8]
    dxn = _mm(dhq, wts['w_qkv_t'], "nn", "l0_in_qkv_dx")
    dxn = _mm(dgate, wts['w_gate_t'], "nn", "l0_in_gate_dx", res=dxn)
    dxn = _mm(dba, wts['w_ba_t'], "nn", "l0_in_ba_dx", res=dxn)
    dxn = _mm(du, wts['w_u_t'], "nn", "l0_in_u_dx", res=dxn)
    dw_qkv_t = _mm(dhq, xn, "tn", "l0_in_qkv_dw", out_dtype=BF16)
    dw_gate_t = _mm(dgate, xn, "tn", "l0_in_gate_dw", out_dtype=BF16)
    dw_ba_t = _mm(dba, xn, "tn", "l0_in_ba_dw", out_dtype=BF16)
    dw_u_t = _mm(du, xn, "tn", "l0_in_u_dw", out_dtype=BF16)
    gr['w_in_t'] = jnp.concatenate([dw_qkv_t, dw_gate_t, dw_ba_t[:8], dw_u_t], axis=0)
    return dxn, gr


def _xa_fwd(x1, g, mem_n, wq, wkv_t, wo, tag):
    xq = _rms_fwd(x1, g, BF16, tag + "_norm")
    q = _mm(xq, wq, "nn", tag + "_q", out_dtype=BF16)
    kv = _mm(mem_n, wkv_t, "nt", tag + "_kv", out_dtype=BF16)
    o = _attn_fwd(q, kv, tag + "_attn")
    x2 = _mm(o, wo, "nn", tag + "_o", res=x1)
    return x2, dict(xq=xq, q=q, kv=kv, o=o)


def _xa_bwd(dx2, x1, g, mem_n, wq, wkv_t, wo, sv, tag):
    do = _mm(dx2, wo, "nt", tag + "_o_dx", out_dtype=BF16)
    dwo = _mm(sv['o'], dx2, "tn", tag + "_o_dw", out_dtype=BF16)
    dq, dk, dv = _attn_bwd(sv['q'], sv['kv'], do, tag + "_attn_bwd")
    dkv = jnp.concatenate([dk, dv], axis=1).astype(BF16)
    dxq = _mm(dq, wq, "nt", tag + "_q_dx")
    dwq = _mm(sv['xq'], dq, "tn", tag + "_q_dw", out_dtype=BF16)
    dmem_n = _mm(dkv, wkv_t, "nn", tag + "_kv_dx")
    dwkv_t = _mm(dkv, mem_n, "tn", tag + "_kv_dw", out_dtype=BF16)
    dx1, dg = _rms_bwd(x1, g, dxq, dx2, tag + "_norm_bwd")
    return dx1, dmem_n, dict(wq=dwq, wkv_t=dwkv_t, wo=dwo, g=dg)


def _ffn_fwd(x2, g, w_up_t, conv, w_down, tag):
    xf = _rms_fwd(x2, g, BF16, tag + "_norm")
    h = _mm(xf, w_up_t, "nt", tag + "_up")
    a = _ffn_act_fwd(h, conv, tag + "_act")
    x3 = _mm(a, w_down, "nn", tag + "_down", res=x2)
    return x3, dict(xf=xf, h=h, a=a)


def _ffn_bwd(dx3, x2, g, w_up_t, conv, w_down, sv, tag):
    da = _mm(dx3, w_down, "nt", tag + "_down_dx")
    dw_down = _mm(sv['a'], dx3, "tn", tag + "_down_dw", out_dtype=BF16)
    dh, dconv = _ffn_act_bwd(sv['h'], conv, da, tag + "_act_bwd")
    dxf = _mm(dh, w_up_t, "nn", tag + "_up_dx")
    dw_up_t = _mm(dh, sv['xf'], "tn", tag + "_up_dw", out_dtype=BF16)
    dx2, dg = _rms_bwd(x2, g, dxf, dx3, tag + "_norm_bwd")
    return dx2, dict(w_up_t=dw_up_t, conv=dconv, w_down=dw_down, g=dg)


BIG_NAMES, SMALL_NAMES, REP_NAMES = list(BIG_SHARDED), list(SMALL_SHARDED), list(REPLICATED)
BIG_SIZES = [int(np.prod(_shard_shape(*BIG_SHARDED[n]))) for n in BIG_NAMES]
SMALL_SIZES = [int(np.prod(_shard_shape(*SMALL_SHARDED[n]))) for n in SMALL_NAMES]


PIECES = [('w_in_t', 384), ('w_glu', 32), ('w_out', 128), ('pool_w', 32), ('wq0', 128), ('wq1', 128),
          ('wkv_t0', 256), ('wkv_t1', 256), ('wo0', 128), ('wo1', 128), ('up_t0', 704), ('up_t1', 704),
          ('down0', 352), ('down1', 352)]
PIECE_OFFS = dict(zip([k for k, _ in PIECES], np.concatenate([[0], np.cumsum([r for _, r in PIECES])[:-1]]).tolist()))
W_IN_ROWS = 4 * WIDTH_A + 2 * N_HEADS_A + SSM_WIDTH
W_IN_PIECE = W_IN_ROWS // N_DEV


def _gather_weights(inp):
    bf = lambda a: a.astype(BF16)
    local = {'w_in_t': bf(inp['w_in_ab'][0]).T, 'w_glu': bf(inp['w_glu_b'][0]), 'w_out': bf(inp['w_out_ab'][0]),
             'pool_w': bf(inp['pool_w'][0])}
    for l in range(2):
        local['wq%d' % l] = bf(inp['xa_wq'][l])
        local['wkv_t%d' % l] = bf(inp['xa_wkv'][l]).T
        local['wo%d' % l] = bf(inp['xa_wo'][l])
        local['up_t%d' % l] = bf(inp['ffn_w_up'][l]).T
        local['down%d' % l] = bf(inp['ffn_w_down'][l])
    small_offs = _row_offsets(SMALL_SIZES)
    small_local = _pack_rows([inp[n].reshape(-1) for n in SMALL_NAMES], 8)
    keys = [k for k, _ in PIECES]
    gathered = _all_gather([local[k] for k in keys] + [small_local], "gather_weights")
    got = dict(zip(keys, gathered[:-1]))
    small_all = gathered[-1]
    rows = lambda a: a.reshape(N_DEV * a.shape[1], a.shape[2])
    full = {'w_in_t': rows(got['w_in_t']), 'w_glu': rows(got['w_glu']), 'w_out': rows(got['w_out']),
            'pool_w': jnp.swapaxes(got['pool_w'], 0, 1).reshape(len(POOL_WINDOWS), POOL_GROUP, POOL_GROUP)}
    for name in ('wq', 'wkv_t', 'wo', 'up_t', 'down'):
        full[name] = [rows(got[name + '0']), rows(got[name + '1'])]
    for n, off, size in zip(SMALL_NAMES, small_offs, SMALL_SIZES):
        r = _round_up(size, PACK_COLS) // PACK_COLS
        full[n] = _merge_shards(small_all[:, off:off + r].reshape(N_DEV, -1)[:, :size], *SMALL_SHARDED[n])
    return full


def _local_step(inp, full):
    f32_of = lambda n: inp[n].astype(F32)
    w_in_t = full['w_in_t']
    wts0 = dict(w_qkv_t=w_in_t[:3 * WIDTH_A], w_gate_t=w_in_t[3 * WIDTH_A:4 * WIDTH_A],
                w_ba_t=jnp.concatenate([w_in_t[4 * WIDTH_A:4 * WIDTH_A + 8], jnp.zeros((LANE - 8, D_MODEL), BF16)], 0),
                w_u_t=w_in_t[4 * WIDTH_A + 8:], w_glu=full['w_glu'], w_out=full['w_out'])
    lb_disc, disc_vjp = jax.vjp(_s5_discretise, f32_of('ssm_lambda_re')[0], f32_of('ssm_lambda_im')[0],
                                f32_of('ssm_b_re')[0], f32_of('ssm_b_im')[0], f32_of('ssm_log_dt')[0])
    b_in, c_out, a_row = _s5_matrices(*lb_disc, f32_of('ssm_c_re')[0], f32_of('ssm_c_im')[0])
    zeros4 = jnp.zeros((1, 4), F32)
    p0 = dict(conv_qkv=full['conv_qkv_a'][0], onorm_g=f32_of('onorm_g_a'),
              arow=jnp.concatenate([zeros4, f32_of('a_log_a'), jnp.zeros((1, LANE - 8), F32)], 1),
              brow=jnp.concatenate([zeros4, f32_of('dt_bias_a'), jnp.zeros((1, LANE - 8), F32)], 1),
              b_in=b_in.astype(BF16), c_out=c_out.astype(BF16), a_row=a_row,
              d_row=f32_of('ssm_d').reshape(1, SSM_WIDTH), b_glu=f32_of('b_glu_b'))

    x0 = inp['x'][0]
    mem_n = _rms_fwd(inp['mem'][0], inp['norm_mem_g'], BF16, "mem_norm")
    xn0 = _rms_fwd(x0, inp['norm_mix_g'][0], BF16, "l0_mix_norm")
    x1, sv_mix0 = _hybrid_fwd(xn0, x0, wts0, p0)
    x2, sv_xa0 = _xa_fwd(x1, inp['norm_xa_g'][0], mem_n, full['wq'][0], full['wkv_t'][0], full['wo'][0], "l0_xa")
    x3, sv_ffn0 = _ffn_fwd(x2, inp['norm_ffn_g'][0], full['up_t'][0], full['ffn_conv'][0], full['down'][0], "l0_ffn")
    xn1 = _rms_fwd(x3, inp['norm_mix_g'][1], F32, "l1_mix_norm")
    x4 = _pool_fwd(xn1, full['pool_w'], full['pool_scale'], x3, "l1_pool")
    x5, sv_xa1 = _xa_fwd(x4, inp['norm_xa_g'][1], mem_n, full['wq'][1], full['wkv_t'][1], full['wo'][1], "l1_xa")
    x6, sv_ffn1 = _ffn_fwd(x5, inp['norm_ffn_g'][1], full['up_t'][1], full['ffn_conv'][1], full['down'][1], "l1_ffn")
    loss_part, dx6, dg_final = _loss_head(x6, inp['norm_final_g'], inp['loss_target'][0], "loss_head")

    dx5, g_ffn1 = _ffn_bwd(dx6, x5, inp['norm_ffn_g'][1], full['up_t'][1], full['ffn_conv'][1], full['down'][1], sv_ffn1, "l1_ffn")
    dx4, dmem1, g_xa1 = _xa_bwd(dx5, x4, inp['norm_xa_g'][1], mem_n, full['wq'][1], full['wkv_t'][1], full['wo'][1], sv_xa1, "l1_xa")
    dxn1, dpool_w, dpool_scale = _pool_bwd(xn1, full['pool_w'], full['pool_scale'], dx4, "l1_pool_bwd")
    dx3, dg_mix1 = _rms_bwd(x3, inp['norm_mix_g'][1], dxn1, dx4, "l1_mix_norm_bwd")
    dx2, g_ffn0 = _ffn_bwd(dx3, x2, inp['norm_ffn_g'][0], full['up_t'][0], full['ffn_conv'][0], full['down'][0], sv_ffn0, "l0_ffn")
    dx1, dmem0, g_xa0 = _xa_bwd(dx2, x1, inp['norm_xa_g'][0], mem_n, full['wq'][0], full['wkv_t'][0], full['wo'][0], sv_xa0, "l0_xa")
    dxn0, g_mix0 = _hybrid_bwd(dx1, xn0, wts0, p0, sv_mix0)
    grad_x, dg_mix0 = _rms_bwd(x0, inp['norm_mix_g'][0], dxn0, dx1, "l0_mix_norm_bwd")
    _, dg_mem = _rms_bwd(inp['mem'][0], inp['norm_mem_g'], dmem0 + dmem1, None, "mem_norm_bwd")

    db_in, dc_out, da_row, dd = g_mix0['s5']
    dlb_re, dlb_im, dbb_re, dbb_im, dc_re, dc_im = _s5_matrix_grads(db_in, dc_out, da_row)
    dlam_re, dlam_im, dbr, dbi, dlog_dt = disc_vjp((dlb_re, dlb_im, dbb_re, dbb_im))

    pieces = lambda a: a.reshape(N_DEV, a.shape[0] // N_DEV, a.shape[1])
    w_in_pieces = pieces(g_mix0['w_in_t'])
    w_in_pieces = jnp.concatenate(
        [w_in_pieces, jnp.zeros((N_DEV, PIECES[0][1] - W_IN_PIECE, D_MODEL), BF16)], axis=1)
    pool_pieces = jnp.swapaxes(dpool_w.astype(BF16).reshape(len(POOL_WINDOWS), N_DEV, -1, POOL_GROUP), 0, 1)
    big_grads = {
        'w_in_t': w_in_pieces, 'w_glu': g_mix0['w_glu_b'].astype(BF16).reshape(N_DEV, -1, PACK_COLS),
        'w_out': pieces(g_mix0['w_out_ab']), 'pool_w': pool_pieces.reshape(N_DEV, -1, PACK_COLS)}
    for l, (gx, gf) in enumerate(((g_xa0, g_ffn0), (g_xa1, g_ffn1))):
        big_grads['wq%d' % l], big_grads['wkv_t%d' % l] = pieces(gx['wq']), pieces(gx['wkv_t'])
        big_grads['wo%d' % l] = pieces(gx['wo'])
        big_grads['up_t%d' % l], big_grads['down%d' % l] = pieces(gf['w_up_t']), pieces(gf['w_down'])
    rep_grads = {
        'norm_mix_g': jnp.concatenate([dg_mix0, dg_mix1], 0), 'norm_xa_g': jnp.concatenate([g_xa0['g'], g_xa1['g']], 0),
        'norm_ffn_g': jnp.concatenate([g_ffn0['g'], g_ffn1['g']], 0), 'norm_mem_g': dg_mem.reshape(-1),
        'norm_final_g': dg_final.reshape(-1), 'a_log_a': g_mix0['a_log_a'], 'dt_bias_a': g_mix0['dt_bias_a'],
        'onorm_g_a': g_mix0['onorm_g_a'], 'ssm_lambda_re': dlam_re[None], 'ssm_lambda_im': dlam_im[None],
        'ssm_b_re': dbr[None], 'ssm_b_im': dbi[None], 'ssm_c_re': dc_re[None], 'ssm_c_im': dc_im[None],
        'ssm_d': dd.reshape(1, N_GROUPS, SSM_GROUP), 'ssm_log_dt': dlog_dt[None], 'b_glu_b': g_mix0['b_glu_b']}
    small_grads = {'conv_qkv_a': g_mix0['conv_qkv_a'][None], 'pool_scale': dpool_scale,
                   'ffn_conv': jnp.stack([g_ffn0['conv'], g_ffn1['conv']])}
    return loss_part, grad_x, big_grads, rep_grads, small_grads


def _reduce_gradients(big_grads, rep_grads, small_grads):
    for key, rows in PIECES:
        assert big_grads[key].shape == (N_DEV, rows, PACK_COLS) and big_grads[key].dtype == BF16, key
    piece_list = [big_grads[k] for k, _ in PIECES]
    core = lax.axis_index("c")
    keep = jnp.concatenate([lax.dynamic_index_in_dim(a.reshape(4, 2, a.shape[1], PACK_COLS), core, 1, keepdims=False)
                            for a in piece_list], axis=1)
    from_sibling = _swap_sibling(piece_list, "grads_to_sibling")
    chip_sums = _pair_sum(keep, from_sibling, "grads_pair_sum")
    from_chips = _exchange_chips(chip_sums, "grads_to_chips")
    big_reduced = _chip_sum(chip_sums, from_chips, "grads_chip_sum")

    misc_list = [rep_grads[n].astype(F32).reshape(-1) for n in REP_NAMES] + \
                [small_grads[n].astype(F32).reshape(-1) for n in SMALL_NAMES]
    misc_sizes = [int(a.shape[0]) for a in misc_list]
    misc_local = _pack_rows(misc_list, 8)
    (misc_all,) = _all_gather([misc_local], "gather_small_grads")
    misc_sum = _sum_leading(misc_all, "small_grads_sum")
    return big_reduced, misc_sum, misc_sizes


def _update(inp, loss_part, grad_x, big_reduced, misc_sum, misc_sizes):
    big_names, small_names, rep_names = BIG_NAMES, SMALL_NAMES, REP_NAMES
    misc_offs = _row_offsets(misc_sizes)
    dev = 4 * lax.axis_index("x") + 2 * lax.axis_index("y") + lax.axis_index("c")
    piece_rows = dict(PIECES)
    piece = lambda key, rows=None: big_reduced[PIECE_OFFS[key]:PIECE_OFFS[key] + (rows or piece_rows[key])]
    both = lambda name, fn: jnp.stack([fn(piece(name + '0')), fn(piece(name + '1'))])
    ident, transpose = (lambda a: a), (lambda a: a.T)
    grads = {'w_in_ab': piece('w_in_t', W_IN_PIECE).T[None], 'w_glu_b': piece('w_glu').reshape(inp['w_glu_b'].shape),
             'w_out_ab': piece('w_out')[None], 'pool_w': piece('pool_w').reshape(inp['pool_w'].shape),
             'xa_wq': both('wq', ident), 'xa_wkv': both('wkv_t', transpose), 'xa_wo': both('wo', ident),
             'ffn_w_up': both('up_t', transpose), 'ffn_w_down': both('down', ident)}
    for n, off, size in zip(rep_names + small_names, misc_offs, misc_sizes):
        flat = misc_sum[off:off + _round_up(size, PACK_COLS) // PACK_COLS].reshape(-1)[:size]
        if n in REPLICATED:
            grads[n] = flat.reshape(inp[n].shape)
        else:
            shape, axis = SMALL_SHARDED[n]
            grads[n] = lax.dynamic_index_in_dim(_split_shards(flat.reshape(shape), axis), dev, 0, keepdims=False
                                                ).reshape(inp[n].shape)
    tiny_names = [n for n in WEIGHT_NAMES if n not in BIG_SHARDED]
    upd = {}
    for n in big_names:
        upd[n] = _adamw(inp[n], grads[n], inp['m_' + n], inp['v_' + n], "adamw_" + n)
    tiny_sizes = [int(np.prod(inp[n].shape)) for n in tiny_names]
    tiny_offs = _row_offsets(tiny_sizes)
    packs = [_pack_rows([src(n).astype(F32).reshape(-1) for n in tiny_names], 8)
             for src in (lambda n: inp[n], lambda n: grads[n], lambda n: inp['m_' + n], lambda n: inp['v_' + n])]
    tiny_out = _adamw(*packs, "adamw_small")
    for n, off, size in zip(tiny_names, tiny_offs, tiny_sizes):
        r = _round_up(size, PACK_COLS) // PACK_COLS
        upd[n] = tuple(o[off:off + r].reshape(-1)[:size].reshape(inp[n].shape) for o in tiny_out)

    loss = lax.psum(loss_part[0, 0], ("x", "y", "c"))
    outs = [loss, grad_x[None]]
    outs += [grads[n] for n in WEIGHT_NAMES]
    for i in range(3):
        outs += [upd[n][i] for n in WEIGHT_NAMES]
    return tuple(outs)


def _step(inp):
    full = _gather_weights(inp)
    loss_part, grad_x, big_grads, rep_grads, small_grads = _local_step(inp, full)
    big_reduced, misc_sum, misc_sizes = _reduce_gradients(big_grads, rep_grads, small_grads)
    return _update(inp, loss_part, grad_x, big_reduced, misc_sum, misc_sizes)


INPUT_NAMES = (['x', 'mem'] + WEIGHT_NAMES + ['loss_target'] + ['m_' + n for n in WEIGHT_NAMES]
               + ['v_' + n for n in WEIGHT_NAMES])


def kernel(x, mem, norm_mix_g, norm_xa_g, norm_ffn_g, norm_mem_g, norm_final_g, w_in_ab, conv_qkv_a, a_log_a, dt_bias_a, onorm_g_a, ssm_lambda_re, ssm_lambda_im, ssm_b_re, ssm_b_im, ssm_c_re, ssm_c_im, ssm_d, ssm_log_dt, w_glu_b, b_glu_b, w_out_ab, pool_w, pool_scale, xa_wq, xa_wkv, xa_wo, ffn_w_up, ffn_conv, ffn_w_down, loss_target, m_norm_mix_g, m_norm_xa_g, m_norm_ffn_g, m_norm_mem_g, m_norm_final_g, m_w_in_ab, m_conv_qkv_a, m_a_log_a, m_dt_bias_a, m_onorm_g_a, m_ssm_lambda_re, m_ssm_lambda_im, m_ssm_b_re, m_ssm_b_im, m_ssm_c_re, m_ssm_c_im, m_ssm_d, m_ssm_log_dt, m_w_glu_b, m_b_glu_b, m_w_out_ab, m_pool_w, m_pool_scale, m_xa_wq, m_xa_wkv, m_xa_wo, m_ffn_w_up, m_ffn_conv, m_ffn_w_down, v_norm_mix_g, v_norm_xa_g, v_norm_ffn_g, v_norm_mem_g, v_norm_final_g, v_w_in_ab, v_conv_qkv_a, v_a_log_a, v_dt_bias_a, v_onorm_g_a, v_ssm_lambda_re, v_ssm_lambda_im, v_ssm_b_re, v_ssm_b_im, v_ssm_c_re, v_ssm_c_im, v_ssm_d, v_ssm_log_dt, v_w_glu_b, v_b_glu_b, v_w_out_ab, v_pool_w, v_pool_scale, v_xa_wq, v_xa_wkv, v_xa_wo, v_ffn_w_up, v_ffn_conv, v_ffn_w_down):
    args = (x, mem, norm_mix_g, norm_xa_g, norm_ffn_g, norm_mem_g, norm_final_g, w_in_ab, conv_qkv_a, a_log_a, dt_bias_a, onorm_g_a, ssm_lambda_re, ssm_lambda_im, ssm_b_re, ssm_b_im, ssm_c_re, ssm_c_im, ssm_d, ssm_log_dt, w_glu_b, b_glu_b, w_out_ab, pool_w, pool_scale, xa_wq, xa_wkv, xa_wo, ffn_w_up, ffn_conv, ffn_w_down, loss_target, m_norm_mix_g, m_norm_xa_g, m_norm_ffn_g, m_norm_mem_g, m_norm_final_g, m_w_in_ab, m_conv_qkv_a, m_a_log_a, m_dt_bias_a, m_onorm_g_a, m_ssm_lambda_re, m_ssm_lambda_im, m_ssm_b_re, m_ssm_b_im, m_ssm_c_re, m_ssm_c_im, m_ssm_d, m_ssm_log_dt, m_w_glu_b, m_b_glu_b, m_w_out_ab, m_pool_w, m_pool_scale, m_xa_wq, m_xa_wkv, m_xa_wo, m_ffn_w_up, m_ffn_conv, m_ffn_w_down, v_norm_mix_g, v_norm_xa_g, v_norm_ffn_g, v_norm_mem_g, v_norm_final_g, v_w_in_ab, v_conv_qkv_a, v_a_log_a, v_dt_bias_a, v_onorm_g_a, v_ssm_lambda_re, v_ssm_lambda_im, v_ssm_b_re, v_ssm_b_im, v_ssm_c_re, v_ssm_c_im, v_ssm_d, v_ssm_log_dt, v_w_glu_b, v_b_glu_b, v_w_out_ab, v_pool_w, v_pool_scale, v_xa_wq, v_xa_wkv, v_xa_wo, v_ffn_w_up, v_ffn_conv, v_ffn_w_down)
    return _step(dict(zip(INPUT_NAMES, args)))
```

```python
import functools
import math

import numpy as np
import jax
import jax.numpy as jnp
from jax import lax
from jax.experimental import pallas as pl
from jax.experimental.pallas import tpu as pltpu

F32, BF16 = jnp.float32, jnp.bfloat16
HIGH, HIGHEST = lax.Precision.HIGH, lax.Precision.HIGHEST
MESH = pl.DeviceIdType.MESH

N_DEV = 8
SEQ, D_MODEL, MEM_LEN = 2048, 1024, 256
WIDTH_A, N_HEADS_A, HEAD_A, CONV_A = 512, 4, 128, 4
GDR_CHUNK = 128
SSM_WIDTH, SSM_GROUP, N_GROUPS, SSM_STATE = 512, 16, 32, 64
SSM_CH = N_GROUPS * SSM_STATE
SCAN_CB = 512
POOL_WINDOWS = (2, 4, 8, 16)
POOL_GROUP = 256
N_HEADS_X, HEAD_X = 4, 256
D_FF, CONV_FFN = 2816, 3
RMS_EPS = 1e-6
ADAM_LR, ADAM_B1, ADAM_B2, ADAM_EPS, ADAM_WD, ADAM_STEP = 0.001, 0.9, 0.999, 1e-08, 0.01, 10
LANE = 128
PACK_COLS = 1024
VMEM_LIMIT_BYTES = 56 * 1024 * 1024


def _params(sem=None):
    return pltpu.CompilerParams(dimension_semantics=sem, vmem_limit_bytes=VMEM_LIMIT_BYTES)


class Comm:
    def __init__(self, inputs, out_shapes, sems, start, end, mid=None):
        self.inputs, self.out_shapes, self.sems = list(inputs), list(out_shapes), list(sems)
        self.start, self.mid, self.end = start, mid, end


def _merge_comms(comms):
    comms = [c for c in comms if c is not None]
    if not comms:
        return None, []
    bounds, ni, no, ns = [], 0, 0, 0
    for c in comms:
        bounds.append((ni, no, ns))
        ni, no, ns = ni + len(c.inputs), no + len(c.out_shapes), ns + len(c.sems)

    def phase(which):
        def run(ins, outs, sems):
            for c, (i0, o0, s0) in zip(comms, bounds):
                fn = getattr(c, which)
                if fn is not None:
                    fn(ins[i0:i0 + len(c.inputs)], outs[o0:o0 + len(c.out_shapes)], sems[s0:s0 + len(c.sems)])
        return run

    merged = Comm([a for c in comms for a in c.inputs], [s for c in comms for s in c.out_shapes],
                  [s for c in comms for s in c.sems], phase("start"), phase("end"), phase("mid"))
    return merged, [(o0, o0 + len(c.out_shapes)) for c, (_, o0, _) in zip(comms, bounds)]


def _call(body, *, name, grid, in_specs, out_specs, out_shape, args, scratch_shapes=(), sem=None, comm=None):
    single = not isinstance(out_shape, (list, tuple))
    out_specs_l = [out_specs] if single else list(out_specs)
    out_shape_l = [out_shape] if single else list(out_shape)
    scratch_shapes = list(scratch_shapes)
    merged, spans = _merge_comms(comm if isinstance(comm, (list, tuple)) else [comm])
    if merged is None:
        outs = pl.pallas_call(body, name=name, grid=grid, in_specs=list(in_specs), out_specs=out_specs_l,
                              out_shape=out_shape_l, scratch_shapes=scratch_shapes, compiler_params=_params(sem))(*args)
        outs = outs[0] if single else outs
        return outs if comm is None else (outs, [])
    n_in, n_out, n_scr = len(in_specs), len(out_specs_l), len(scratch_shapes)
    ci, co = len(merged.inputs), len(merged.out_shapes)
    total = int(np.prod(grid))

    def wrapped(*refs):
        ins, cins = refs[:n_in], refs[n_in:n_in + ci]
        outs, couts = refs[n_in + ci:n_in + ci + n_out], refs[n_in + ci + n_out:n_in + ci + n_out + co]
        scr, csems = refs[n_in + ci + n_out + co:n_in + ci + n_out + co + n_scr], refs[n_in + ci + n_out + co + n_scr:]
        lin = pl.program_id(0)
        for d in range(1, len(grid)):
            lin = lin * grid[d] + pl.program_id(d)
        pl.when(lin == 0)(lambda: merged.start(cins, couts, csems))
        body(*ins, *outs, *scr)
        pl.when(lin == total // 2)(lambda: merged.mid(cins, couts, csems))
        pl.when(lin == total - 1)(lambda: merged.end(cins, couts, csems))

    any_spec = pl.BlockSpec(memory_space=pl.ANY)
    res = pl.pallas_call(
        wrapped, name=name, grid=grid, in_specs=list(in_specs) + [any_spec] * ci,
        out_specs=out_specs_l + [any_spec] * co, out_shape=out_shape_l + merged.out_shapes,
        scratch_shapes=scratch_shapes + merged.sems,
        compiler_params=_params(("arbitrary",) * len(grid)))(*args, *merged.inputs)
    outs, couts = res[:n_out], res[n_out:]
    return (outs[0] if single else list(outs)), [list(couts[a:b]) for a, b in spans]


def _comm_only(comm, name):
    def body():
        pass

    _, couts = _call(body, name=name, grid=(1,), in_specs=[], out_specs=[], out_shape=[], args=[], comm=comm)
    return couts[0]


def _tile(dim, pref):
    best = None
    for t in range(LANE, min(dim, pref) + 1, LANE):
        if dim % t == 0:
            best = t
    return best if best is not None else dim


MM_VMEM_BUDGET = 40 * 1024 * 1024


def _mm_tiles(m, n, k, a_bytes, b_bytes, o_bytes, r_bytes):
    for tk in (k, _tile(k, 2048), _tile(k, 1024), _tile(k, 512)):
        for tm, tn in ((1024, 1536), (1024, 1024), (1024, 512), (512, 512), (256, 512), (256, 256)):
            tm, tn = _tile(m, tm), _tile(n, tn)
            acc = 0 if tk == k else tm * tn * 4
            need = 2 * (tm * tk * a_bytes + tk * tn * b_bytes + tm * tn * (o_bytes + r_bytes)) + acc
            if need <= MM_VMEM_BUDGET:
                return tm, tn, tk
    raise ValueError("no matmul tiling fits VMEM")


def _mm(a, b, mode, name, out_dtype=F32, res=None, comm=None):
    if mode == "nn":
        (m, k), n = a.shape, b.shape[1]
    elif mode == "nt":
        (m, k), n = a.shape, b.shape[0]
    else:
        (k, m), n = a.shape, b.shape[1]
    tm, tn, tk = _mm_tiles(m, n, k, a.dtype.itemsize, b.dtype.itemsize, jnp.dtype(out_dtype).itemsize,
                           0 if res is None else res.dtype.itemsize)
    nk = k // tk
    dims = {"nn": ((1,), (0,)), "nt": ((1,), (1,)), "tn": ((0,), (0,))}[mode]

    def body(*refs):
        if res is None:
            a_ref, b_ref, o_ref = refs[:3]
            r_ref = None
        else:
            a_ref, b_ref, r_ref, o_ref = refs[:4]
        part = lax.dot_general(a_ref[...].astype(BF16), b_ref[...].astype(BF16), (dims, ((), ())),
                               preferred_element_type=F32)

        def finish(out):
            if r_ref is not None:
                out = out + r_ref[...].astype(F32)
            o_ref[...] = out.astype(out_dtype)

        if nk == 1:
            finish(part)
            return
        acc = refs[-1]
        kk = pl.program_id(2)

        @pl.when(kk == 0)
        def _():
            acc[...] = part

        @pl.when(kk > 0)
        def _():
            acc[...] += part

        @pl.when(kk == nk - 1)
        def _():
            finish(acc[...])

    a_spec = (pl.BlockSpec((tk, tm), lambda i, j, q: (q, i)) if mode == "tn"
              else pl.BlockSpec((tm, tk), lambda i, j, q: (i, q)))
    b_spec = (pl.BlockSpec((tn, tk), lambda i, j, q: (j, q)) if mode == "nt"
              else pl.BlockSpec((tk, tn), lambda i, j, q: (q, j)))
    o_spec = pl.BlockSpec((tm, tn), lambda i, j, q: (i, j))
    in_specs, args = [a_spec, b_spec], [a, b]
    if res is not None:
        in_specs.append(o_spec)
        args.append(res)
    return _call(body, name=name, grid=(m // tm, n // tn, nk), in_specs=in_specs, out_specs=o_spec,
                 out_shape=jax.ShapeDtypeStruct((m, n), out_dtype),
                 scratch_shapes=[] if nk == 1 else [pltpu.VMEM((tm, tn), F32)],
                 sem=("parallel", "parallel", "arbitrary"), args=args, comm=comm)


def _rms_fwd(x, g, out_dtype, name, tr=256):
    rows, d = x.shape

    def body(x_ref, g_ref, o_ref):
        xv = x_ref[...]
        r = lax.rsqrt(jnp.mean(xv * xv, axis=-1, keepdims=True) + RMS_EPS)
        o_ref[...] = (xv * r * g_ref[...]).astype(out_dtype)

    return pl.pallas_call(
        body, name=name, grid=(rows // tr,),
        in_specs=[pl.BlockSpec((tr, d), lambda i: (i, 0)), pl.BlockSpec((1, d), lambda i: (0, 0))],
        out_specs=pl.BlockSpec((tr, d), lambda i: (i, 0)), out_shape=jax.ShapeDtypeStruct((rows, d), out_dtype),
        compiler_params=_params(("parallel",)))(x, g.reshape(1, d))


def _rms_bwd(x, g, dy, dres, name, tr=256):
    rows, d = x.shape

    def body(*refs):
        if dres is None:
            x_ref, g_ref, dy_ref, dx_ref, dg_ref = refs
            r_ref = None
        else:
            x_ref, g_ref, dy_ref, r_ref, dx_ref, dg_ref = refs

        @pl.when(pl.program_id(0) == 0)
        def _():
            dg_ref[...] = jnp.zeros_like(dg_ref)

        xv, dyv = x_ref[...], dy_ref[...].astype(F32)
        r = lax.rsqrt(jnp.mean(xv * xv, axis=-1, keepdims=True) + RMS_EPS)
        xh = xv * r
        dyg = dyv * g_ref[...]
        dx = r * (dyg - xh * jnp.mean(dyg * xh, axis=-1, keepdims=True))
        if r_ref is not None:
            dx = dx + r_ref[...]
        dx_ref[...] = dx
        dg_ref[...] += jnp.sum(dyv * xh, axis=0, keepdims=True)

    blk = pl.BlockSpec((tr, d), lambda i: (i, 0))
    vec = pl.BlockSpec((1, d), lambda i: (0, 0))
    in_specs, args = [blk, vec, blk], [x, g.reshape(1, d), dy]
    if dres is not None:
        in_specs.append(blk)
        args.append(dres)
    return pl.pallas_call(
        body, name=name, grid=(rows // tr,), in_specs=in_specs, out_specs=[blk, vec],
        out_shape=[jax.ShapeDtypeStruct((rows, d), F32), jax.ShapeDtypeStruct((1, d), F32)],
        compiler_params=_params(("arbitrary",)))(*args)


def _loss_head(x, g, target, name, tr=256):
    rows, d = x.shape

    def body(x_ref, g_ref, t_ref, loss_ref, dx_ref, dg_ref):
        @pl.when(pl.program_id(0) == 0)
        def _():
            dg_ref[...] = jnp.zeros_like(dg_ref)
            loss_ref[...] = jnp.zeros_like(loss_ref)

        xv = x_ref[...]
        r = lax.rsqrt(jnp.mean(xv * xv, axis=-1, keepdims=True) + RMS_EPS)
        xh = xv * r
        err = xh * g_ref[...] - t_ref[...]
        loss_ref[...] += 0.5 * jnp.sum(jnp.mean(err * err, axis=-1, keepdims=True), keepdims=True)
        dyv = err * (1.0 / d)
        dyg = dyv * g_ref[...]
        dx_ref[...] = r * (dyg - xh * jnp.mean(dyg * xh, axis=-1, keepdims=True))
        dg_ref[...] += jnp.sum(dyv * xh, axis=0, keepdims=True)

    blk = pl.BlockSpec((tr, d), lambda i: (i, 0))
    vec = pl.BlockSpec((1, d), lambda i: (0, 0))
    return pl.pallas_call(
        body, name=name, grid=(rows // tr,), in_specs=[blk, vec, blk],
        out_specs=[pl.BlockSpec((1, 1), lambda i: (0, 0)), blk, vec],
        out_shape=[jax.ShapeDtypeStruct((1, 1), F32), jax.ShapeDtypeStruct((rows, d), F32),
                   jax.ShapeDtypeStruct((1, d), F32)],
        compiler_params=_params(("arbitrary",)))(x, g.reshape(1, d), target)


def _shift_down(x, s):
    rows = lax.broadcasted_iota(jnp.int32, x.shape, 0)
    return jnp.where(rows >= s, pltpu.roll(x, s, 0), 0.0)


def _shift_up(x, s):
    n = x.shape[0]
    rows = lax.broadcasted_iota(jnp.int32, x.shape, 0)
    return jnp.where(rows < n - s, pltpu.roll(x, n - s, 0), 0.0)


def _sigmoid(x):
    return 1.0 / (1.0 + jnp.exp(-x))


def _silu_and_grad(x):
    s = _sigmoid(x)
    return x * s, s * (1.0 + x * (1.0 - s))


_GELU_C0, _GELU_C1 = math.sqrt(2.0 / math.pi), 0.044715


def _gelu_and_grad(x):
    th = jnp.tanh(_GELU_C0 * (x + _GELU_C1 * x * x * x))
    y = 0.5 * x * (1.0 + th)
    dy = 0.5 * (1.0 + th) + 0.5 * x * (1.0 - th * th) * _GELU_C0 * (1.0 + 3.0 * _GELU_C1 * x * x)
    return y, dy


def _ffn_act_fwd(h, w, name, tc=256, comm=None):
    t = h.shape[0]
    nb = D_FF // tc

    def body(hg_ref, hv_ref, wg_ref, wv_ref, a_ref):
        def conv(x, wr):
            return wr[2:3, :] * x + wr[1:2, :] * _shift_down(x, 1) + wr[0:1, :] * _shift_down(x, 2)

        cg = conv(hg_ref[...], wg_ref[...])
        cv = conv(hv_ref[...], wv_ref[...])
        a_ref[...] = (cg * _sigmoid(cg) * cv).astype(BF16)

    return _call(
        body, name=name, grid=(nb,),
        in_specs=[pl.BlockSpec((t, tc), lambda j: (0, j)), pl.BlockSpec((t, tc), lambda j: (0, j + nb)),
                  pl.BlockSpec((CONV_FFN, tc), lambda j: (0, j)), pl.BlockSpec((CONV_FFN, tc), lambda j: (0, j + nb))],
        out_specs=pl.BlockSpec((t, tc), lambda j: (0, j)), out_shape=jax.ShapeDtypeStruct((t, D_FF), BF16),
        sem=("parallel",), args=(h, h, w, w), comm=comm)


def _ffn_act_bwd(h, w, da, name, tc=256):
    t = h.shape[0]
    nb = D_FF // tc

    def body(hg_ref, hv_ref, wg_ref, wv_ref, da_ref, dhg_ref, dhv_ref, dwg_ref, dwv_ref):
        hg, hv, wg, wv = hg_ref[...], hv_ref[...], wg_ref[...], wv_ref[...]
        hg1, hg2, hv1, hv2 = _shift_down(hg, 1), _shift_down(hg, 2), _shift_down(hv, 1), _shift_down(hv, 2)
        cg = wg[2:3, :] * hg + wg[1:2, :] * hg1 + wg[0:1, :] * hg2
        cv = wv[2:3, :] * hv + wv[1:2, :] * hv1 + wv[0:1, :] * hv2
        sg, dsg = _silu_and_grad(cg)
        dav = da_ref[...].astype(F32)
        dcv = dav * sg
        dcg = dav * cv * dsg

        def conv_t(dc, wr):
            return wr[2:3, :] * dc + wr[1:2, :] * _shift_up(dc, 1) + wr[0:1, :] * _shift_up(dc, 2)

        dhg_ref[...] = conv_t(dcg, wg).astype(BF16)
        dhv_ref[...] = conv_t(dcv, wv).astype(BF16)
        dwg_ref[0:1, :] = jnp.sum(dcg * hg2, axis=0, keepdims=True)
        dwg_ref[1:2, :] = jnp.sum(dcg * hg1, axis=0, keepdims=True)
        dwg_ref[2:3, :] = jnp.sum(dcg * hg, axis=0, keepdims=True)
        dwv_ref[0:1, :] = jnp.sum(dcv * hv2, axis=0, keepdims=True)
        dwv_ref[1:2, :] = jnp.sum(dcv * hv1, axis=0, keepdims=True)
        dwv_ref[2:3, :] = jnp.sum(dcv * hv, axis=0, keepdims=True)

    big = lambda off: pl.BlockSpec((t, tc), lambda j: (0, j + off))
    small = lambda off: pl.BlockSpec((CONV_FFN, tc), lambda j: (0, j + off))
    dhg, dhv, dwg, dwv = pl.pallas_call(
        body, name=name, grid=(nb,),
        in_specs=[big(0), big(nb), small(0), small(nb), big(0)],
        out_specs=[big(0), big(0), small(0), small(0)],
        out_shape=[jax.ShapeDtypeStruct((t, D_FF), BF16), jax.ShapeDtypeStruct((t, D_FF), BF16),
                   jax.ShapeDtypeStruct((CONV_FFN, D_FF), F32), jax.ShapeDtypeStruct((CONV_FFN, D_FF), F32)],
        compiler_params=_params(("parallel",)))(h, h, w, w, da)
    return jnp.concatenate([dhg, dhv], axis=1), jnp.concatenate([dwg, dwv], axis=1)


def _attn_probs(q, k):
    s = lax.dot_general(q.astype(BF16), k.astype(BF16), (((1,), (1,)), ((), ())),
                        preferred_element_type=F32) * (HEAD_X ** -0.5)
    s = s - jnp.max(s, axis=-1, keepdims=True)
    p = jnp.exp(s)
    return p / jnp.sum(p, axis=-1, keepdims=True)


def _attn_fwd(q, kv, name, tq=512):
    t = q.shape[0]

    def body(q_ref, k_ref, v_ref, o_ref):
        p = _attn_probs(q_ref[...], k_ref[...])
        o_ref[...] = jnp.dot(p.astype(BF16), v_ref[...].astype(BF16), preferred_element_type=F32).astype(BF16)

    return pl.pallas_call(
        body, name=name, grid=(N_HEADS_X, t // tq),
        in_specs=[pl.BlockSpec((tq, HEAD_X), lambda h, i: (i, h)),
                  pl.BlockSpec((MEM_LEN, HEAD_X), lambda h, i: (0, h)),
                  pl.BlockSpec((MEM_LEN, HEAD_X), lambda h, i: (0, h + N_HEADS_X))],
        out_specs=pl.BlockSpec((tq, HEAD_X), lambda h, i: (i, h)),
        out_shape=jax.ShapeDtypeStruct((t, N_HEADS_X * HEAD_X), BF16),
        compiler_params=_params(("parallel", "parallel")))(q, kv, kv)


def _attn_bwd(q, kv, do, name, tq=512):
    t = q.shape[0]

    def body(q_ref, k_ref, v_ref, do_ref, dq_ref, dk_ref, dv_ref):
        @pl.when(pl.program_id(1) == 0)
        def _():
            dk_ref[...] = jnp.zeros_like(dk_ref)
            dv_ref[...] = jnp.zeros_like(dv_ref)

        qb, kb, vb, dob = (r[...].astype(BF16) for r in (q_ref, k_ref, v_ref, do_ref))
        p = _attn_probs(qb, kb)
        dp = lax.dot_general(dob, vb, (((1,), (1,)), ((), ())), preferred_element_type=F32)
        ds = p * (dp - jnp.sum(dp * p, axis=-1, keepdims=True)) * (HEAD_X ** -0.5)
        dsb = ds.astype(BF16)
        dq_ref[...] = jnp.dot(dsb, kb, preferred_element_type=F32).astype(BF16)
        dk_ref[...] += lax.dot_general(dsb, qb, (((0,), (0,)), ((), ())), preferred_element_type=F32)
        dv_ref[...] += lax.dot_general(p.astype(BF16), dob, (((0,), (0,)), ((), ())), preferred_element_type=F32)

    qs = pl.BlockSpec((tq, HEAD_X), lambda h, i: (i, h))
    ms = pl.BlockSpec((MEM_LEN, HEAD_X), lambda h, i: (0, h))
    return pl.pallas_call(
        body, name=name, grid=(N_HEADS_X, t // tq),
        in_specs=[qs, ms, pl.BlockSpec((MEM_LEN, HEAD_X), lambda h, i: (0, h + N_HEADS_X)), qs],
        out_specs=[qs, ms, ms],
        out_shape=[jax.ShapeDtypeStruct((t, D_MODEL), BF16), jax.ShapeDtypeStruct((MEM_LEN, D_MODEL), F32),
                   jax.ShapeDtypeStruct((MEM_LEN, D_MODEL), F32)],
        compiler_params=_params(("parallel", "arbitrary")))(q, kv, kv, do)


def _pool_counts(t, win):
    pos = lax.broadcasted_iota(jnp.int32, (t, 1), 0).astype(F32) + 1.0
    return 1.0 / jnp.minimum(pos, float(win))


def _pool_delta(xv, win):
    s, step = xv, 1
    while step < win:
        s = s + _shift_down(s, step)
        step *= 2
    return s * _pool_counts(xv.shape[0], win) - xv


def _pool_delta_t(dv, win):
    s, step = dv * _pool_counts(dv.shape[0], win), 1
    while step < win:
        s = s + _shift_up(s, step)
        step *= 2
    return s - dv


def _pool_fwd(xn, w, scale, res, name):
    t = xn.shape[0]

    def make_branch(win, xn_ref, w_ref, s_ref, r_ref, o_ref):
        def branch():
            dl = _pool_delta(xn_ref[...], win)
            y = jnp.dot(dl.astype(BF16), w_ref[0], preferred_element_type=F32)
            o_ref[...] = r_ref[...] + y * s_ref[...]
        return branch

    def body(xn_ref, w_ref, s_ref, r_ref, o_ref):
        for gi, win in enumerate(POOL_WINDOWS):
            pl.when(pl.program_id(0) == gi)(make_branch(win, xn_ref, w_ref, s_ref, r_ref, o_ref))

    blk = pl.BlockSpec((t, POOL_GROUP), lambda g: (0, g))
    return pl.pallas_call(
        body, name=name, grid=(len(POOL_WINDOWS),),
        in_specs=[blk, pl.BlockSpec((1, POOL_GROUP, POOL_GROUP), lambda g: (g, 0, 0)),
                  pl.BlockSpec((1, POOL_GROUP), lambda g: (0, g)), blk],
        out_specs=blk, out_shape=jax.ShapeDtypeStruct((t, D_MODEL), F32),
        compiler_params=_params(("parallel",)))(xn, w, scale, res)


def _pool_bwd(xn, w, scale, dmix, name):
    t = xn.shape[0]

    def make_branch(win, xn_ref, w_ref, s_ref, d_ref, dxn_ref, dw_ref, ds_ref):
        def branch():
            dl = _pool_delta(xn_ref[...], win).astype(BF16)
            wv = w_ref[0]
            dm = d_ref[...]
            y = jnp.dot(dl, wv, preferred_element_type=F32)
            ds_ref[...] = jnp.sum(dm * y, axis=0, keepdims=True)
            dy = (dm * s_ref[...]).astype(BF16)
            dw_ref[0] = lax.dot_general(dl, dy, (((0,), (0,)), ((), ())), preferred_element_type=F32)
            ddl = lax.dot_general(dy, wv, (((1,), (1,)), ((), ())), preferred_element_type=F32)
            dxn_ref[...] = _pool_delta_t(ddl, win)
        return branch

    def body(*refs):
        for gi, win in enumerate(POOL_WINDOWS):
            pl.when(pl.program_id(0) == gi)(make_branch(win, *refs))

    blk = pl.BlockSpec((t, POOL_GROUP), lambda g: (0, g))
    wspec = pl.BlockSpec((1, POOL_GROUP, POOL_GROUP), lambda g: (g, 0, 0))
    vec = pl.BlockSpec((1, POOL_GROUP), lambda g: (0, g))
    return pl.pallas_call(
        body, name=name, grid=(len(POOL_WINDOWS),), in_specs=[blk, wspec, vec, blk], out_specs=[blk, wspec, vec],
        out_shape=[jax.ShapeDtypeStruct((t, D_MODEL), F32),
                   jax.ShapeDtypeStruct((len(POOL_WINDOWS), POOL_GROUP, POOL_GROUP), F32),
                   jax.ShapeDtypeStruct((1, D_MODEL), F32)],
        compiler_params=_params(("parallel",)))(xn, w, scale, dmix)


def _qkv_conv(h, wr):
    return (wr[3:4, :] * h + wr[2:3, :] * _shift_down(h, 1) + wr[1:2, :] * _shift_down(h, 2)
            + wr[0:1, :] * _shift_down(h, 3))


def _qkv_pre_fwd(h, w, col0, ncols, normalize, scale, name):
    t = h.shape[0]

    def body(h_ref, w_ref, o_ref):
        c = _qkv_conv(h_ref[...], w_ref[...])
        s = c * _sigmoid(c)
        if normalize:
            s = s * lax.rsqrt(jnp.sum(s * s, axis=-1, keepdims=True) + 1e-6) * scale
        o_ref[...] = s

    return pl.pallas_call(
        body, name=name, grid=(ncols,),
        in_specs=[pl.BlockSpec((t, HEAD_A), lambda j: (0, j + col0)), pl.BlockSpec((CONV_A, HEAD_A), lambda j: (0, j + col0))],
        out_specs=pl.BlockSpec((t, HEAD_A), lambda j: (0, j)), out_shape=jax.ShapeDtypeStruct((t, ncols * HEAD_A), F32),
        compiler_params=_params(("parallel",)))(h, w)


def _qkv_pre_bwd(h, w, dy, col0, ncols, normalize, scale, name):
    t = h.shape[0]

    def body(h_ref, w_ref, dy_ref, dh_ref, dw_ref):
        hv, wr, dyv = h_ref[...], w_ref[...], dy_ref[...]
        h1, h2, h3 = _shift_down(hv, 1), _shift_down(hv, 2), _shift_down(hv, 3)
        c = wr[3:4, :] * hv + wr[2:3, :] * h1 + wr[1:2, :] * h2 + wr[0:1, :] * h3
        s, dsilu = _silu_and_grad(c)
        if normalize:
            r = lax.rsqrt(jnp.sum(s * s, axis=-1, keepdims=True) + 1e-6)
            y = s * r
            dyv = dyv * scale
            ds = r * (dyv - y * jnp.sum(dyv * y, axis=-1, keepdims=True))
        else:
            ds = dyv
        dc = ds * dsilu
        dh = (wr[3:4, :] * dc + wr[2:3, :] * _shift_up(dc, 1) + wr[1:2, :] * _shift_up(dc, 2)
              + wr[0:1, :] * _shift_up(dc, 3))
        dh_ref[...] = dh.astype(BF16)
        dw_ref[0:1, :] = jnp.sum(dc * h3, axis=0, keepdims=True)
        dw_ref[1:2, :] = jnp.sum(dc * h2, axis=0, keepdims=True)
        dw_ref[2:3, :] = jnp.sum(dc * h1, axis=0, keepdims=True)
        dw_ref[3:4, :] = jnp.sum(dc * hv, axis=0, keepdims=True)

    return pl.pallas_call(
        body, name=name, grid=(ncols,),
        in_specs=[pl.BlockSpec((t, HEAD_A), lambda j: (0, j + col0)), pl.BlockSpec((CONV_A, HEAD_A), lambda j: (0, j + col0)),
                  pl.BlockSpec((t, HEAD_A), lambda j: (0, j))],
        out_specs=[pl.BlockSpec((t, HEAD_A), lambda j: (0, j)), pl.BlockSpec((CONV_A, HEAD_A), lambda j: (0, j))],
        out_shape=[jax.ShapeDtypeStruct((t, ncols * HEAD_A), BF16), jax.ShapeDtypeStruct((CONV_A, ncols * HEAD_A), F32)],
        compiler_params=_params(("parallel",)))(h, w, dy)


def _softplus(x):
    return jnp.maximum(x, 0.0) + jnp.log1p(jnp.exp(-jnp.abs(x)))


def _gates_fwd(ba, arow, brow, name):
    t = ba.shape[0]

    def body(x_ref, a_ref, b_ref, o_ref):
        xv = x_ref[...]
        lane = lax.broadcasted_iota(jnp.int32, xv.shape, 1)
        beta = _sigmoid(xv)
        g = -jnp.exp(a_ref[...]) * _softplus(xv + b_ref[...])
        o_ref[...] = jnp.where(lane < N_HEADS_A, beta, jnp.where(lane < 2 * N_HEADS_A, g, 0.0))

    return pl.pallas_call(body, name=name, out_shape=jax.ShapeDtypeStruct((t, LANE), F32),
                          compiler_params=_params())(ba, arow, brow)


def _gates_bwd(ba, arow, brow, dgb, name):
    t = ba.shape[0]

    def body(x_ref, a_ref, b_ref, d_ref, dx_ref, da_ref, db_ref):
        xv, dv = x_ref[...], d_ref[...]
        lane = lax.broadcasted_iota(jnp.int32, xv.shape, 1)
        beta = _sigmoid(xv)
        ea = jnp.exp(a_ref[...])
        z = xv + b_ref[...]
        dgv = jnp.where((lane >= N_HEADS_A) & (lane < 2 * N_HEADS_A), dv, 0.0) * (-ea)
        dz = dgv * _sigmoid(z)
        dx = jnp.where(lane < N_HEADS_A, dv * beta * (1.0 - beta), dz)
        dx_ref[...] = dx.astype(BF16)
        db_ref[...] = jnp.sum(dz, axis=0, keepdims=True)
        da_ref[...] = jnp.sum(dgv * _softplus(z), axis=0, keepdims=True)

    return pl.pallas_call(
        body, name=name,
        out_shape=[jax.ShapeDtypeStruct((t, LANE), BF16), jax.ShapeDtypeStruct((1, LANE), F32),
                   jax.ShapeDtypeStruct((1, LANE), F32)],
        compiler_params=_params())(ba, arow, brow, dgb)


def _dot(a, b, prec=None):
    if prec is None:
        return jnp.dot(a.astype(BF16), b.astype(BF16), preferred_element_type=F32)
    return jnp.dot(a, b, precision=prec, preferred_element_type=F32)


def _dot_nt(a, b, prec=None):
    if prec is None:
        a, b = a.astype(BF16), b.astype(BF16)
    return lax.dot_general(a, b, (((1,), (1,)), ((), ())), precision=prec, preferred_element_type=F32)


def _dot_tn(a, b, prec=None):
    if prec is None:
        a, b = a.astype(BF16), b.astype(BF16)
    return lax.dot_general(a, b, (((0,), (0,)), ((), ())), precision=prec, preferred_element_type=F32)


def _gdr_chunk_terms(k, beta, g):
    c = GDR_CHUNK
    row = lax.broadcasted_iota(jnp.int32, (c, c), 0)
    col = lax.broadcasted_iota(jnp.int32, (c, c), 1)
    causal, strict = row >= col, row > col
    gcum = _dot(causal.astype(F32), jnp.broadcast_to(g, (c, c)), HIGHEST)
    diff = gcum - gcum.T
    decay = jnp.where(causal, jnp.exp(jnp.where(causal, diff, 0.0)), 0.0)
    kb = k * beta
    kk = _dot_nt(kb, k)
    return row, col, causal, strict, gcum, decay, kb, kk


def _unit_lower_inverse(a):
    c = a.shape[0]
    eye = (lax.broadcasted_iota(jnp.int32, (c, c), 0) == lax.broadcasted_iota(jnp.int32, (c, c), 1)).astype(F32)
    p = -a
    inv = eye + p
    step = 1
    while 2 * step < c:
        p = _dot(p, p, HIGH)
        inv = inv + _dot(inv, p, HIGH)
        step *= 2
    return inv


def _gdr_fwd(q, k, v, gb, name, comm=None):
    t = q.shape[0]
    c = GDR_CHUNK
    n = t // c

    def body(q_ref, k_ref, v_ref, gb_ref, o_ref, tm_ref, s_ref, state):
        @pl.when(pl.program_id(1) == 0)
        def _():
            state[...] = jnp.zeros_like(state)

        qv, kv, vv = q_ref[...], k_ref[...], v_ref[...]
        beta, g = gb_ref[0, :, 0:1], gb_ref[0, :, 1:2]
        row, col, causal, strict, gcum, decay, kb, kk = _gdr_chunk_terms(kv, beta, g)
        tm = _unit_lower_inverse(jnp.where(strict, kk * decay, 0.0))
        e = jnp.exp(gcum)
        u = _dot(tm, vv * beta, HIGH)
        w = _dot(tm, kb * e, HIGH)
        p = jnp.where(causal, _dot_nt(qv, kv) * decay, 0.0)
        s = state[...]
        s_ref[0, 0] = s
        tm_ref[0, 0] = tm
        vn = u - _dot(w, s)
        o_ref[...] = _dot(qv * e, s) + _dot(p, vn)
        glast = gcum[c - 1:c, :]
        state[...] = s * jnp.exp(glast) + _dot_tn(kv * jnp.exp(glast - gcum), vn)

    blk = pl.BlockSpec((c, HEAD_A), lambda h, i: (i, h))
    mat = pl.BlockSpec((1, 1, c, c), lambda h, i: (h, i, 0, 0))
    return _call(
        body, name=name, grid=(N_HEADS_A, n),
        in_specs=[blk, blk, blk, pl.BlockSpec((1, c, 2), lambda h, i: (h, i, 0))],
        out_specs=[blk, mat, mat],
        out_shape=[jax.ShapeDtypeStruct((t, WIDTH_A), F32), jax.ShapeDtypeStruct((N_HEADS_A, n, c, c), F32),
                   jax.ShapeDtypeStruct((N_HEADS_A, n, HEAD_A, HEAD_A), F32)],
        scratch_shapes=[pltpu.VMEM((HEAD_A, HEAD_A), F32)], sem=("parallel", "arbitrary"),
        args=(q, k, v, gb), comm=comm)


def _gdr_bwd(q, k, v, gb, tm_all, s_all, do, name):
    t = q.shape[0]
    c = GDR_CHUNK
    n = t // c

    def body(q_ref, k_ref, v_ref, gb_ref, tm_ref, s_ref, do_ref, dq_ref, dk_ref, dv_ref, dgb_ref, dstate):
        @pl.when(pl.program_id(1) == 0)
        def _():
            dstate[...] = jnp.zeros_like(dstate)

        qv, kv, vv, dov = q_ref[...], k_ref[...], v_ref[...], do_ref[...]
        beta, g = gb_ref[0, :, 0:1], gb_ref[0, :, 1:2]
        tm, s, dsp = tm_ref[0, 0], s_ref[0, 0], dstate[...]
        row, col, causal, strict, gcum, decay, kb, kk = _gdr_chunk_terms(kv, beta, g)
        e = jnp.exp(gcum)
        vb, kbe = vv * beta, kb * e
        u = _dot(tm, vb, HIGH)
        w = _dot(tm, kbe, HIGH)
        qk = _dot_nt(qv, kv)
        p = jnp.where(causal, qk * decay, 0.0)
        vn = u - _dot(w, s)
        glast = gcum[c - 1:c, :]
        el = jnp.exp(glast)
        f = jnp.exp(glast - gcum)
        kd = kv * f
        qe = qv * e

        dvn = _dot_tn(p, dov) + _dot(kd, dsp)
        dglast = el[:, 0:1] * jnp.sum(s * dsp, keepdims=True)
        dkd = _dot_nt(vn, dsp)
        dk = dkd * f
        df = jnp.sum(dkd * kv, axis=1, keepdims=True) * f[:, 0:1]
        dglast = dglast + jnp.sum(df, keepdims=True)
        dgc = -df
        dp = jnp.where(causal, _dot_nt(dov, vn), 0.0)
        dqe = _dot_nt(dov, s)
        dq = dqe * e
        de = jnp.sum(dqe * qv, axis=1, keepdims=True)
        dstate[...] = dsp * el + _dot_tn(qe, dov) - _dot_tn(w, dvn)
        dw = -_dot_nt(dvn, s)
        dvb = _dot_tn(tm, dvn, HIGH)
        dkbe = _dot_tn(tm, dw, HIGH)
        da = -jnp.where(strict, _dot_nt(dvb, u) + _dot_nt(dkbe, w), 0.0)
        dkk = da * decay
        dqk = dp * decay
        dd = da * kk + dp * qk
        dq = dq + _dot(dqk, kv)
        dk = dk + _dot_tn(dqk, qv)
        dkb = _dot(dkk, kv) + dkbe * e
        dk = dk + _dot_tn(dkk, kb)
        de = de + jnp.sum(dkbe * kb, axis=1, keepdims=True)
        dk = dk + dkb * beta
        dbeta = jnp.sum(dkb * kv, axis=1, keepdims=True) + jnp.sum(dvb * vv, axis=1, keepdims=True)
        m = dd * decay
        dgc = dgc + jnp.sum(m, axis=1, keepdims=True) - jnp.sum(m.T, axis=1, keepdims=True)
        dgc = dgc + de * e[:, 0:1]
        dgc = dgc + jnp.where(row[:, 0:1] == c - 1, dglast, 0.0)
        dg = _dot((row <= col).astype(F32), jnp.broadcast_to(dgc, (c, c)), HIGHEST)
        dq_ref[...] = dq
        dk_ref[...] = dk
        dv_ref[...] = dvb * beta
        dgb_ref[0, :, 0:1] = dbeta
        dgb_ref[0, :, 1:2] = dg[:, 0:1]

    blk = pl.BlockSpec((c, HEAD_A), lambda h, i: (n - 1 - i, h))
    mat = pl.BlockSpec((1, 1, c, c), lambda h, i: (h, n - 1 - i, 0, 0))
    gsp = pl.BlockSpec((1, c, 2), lambda h, i: (h, n - 1 - i, 0))
    return pl.pallas_call(
        body, name=name, grid=(N_HEADS_A, n),
        in_specs=[blk, blk, blk, gsp, mat, mat, blk], out_specs=[blk, blk, blk, gsp],
        out_shape=[jax.ShapeDtypeStruct((t, WIDTH_A), F32)] * 3 + [jax.ShapeDtypeStruct((N_HEADS_A, t, 2), F32)],
        scratch_shapes=[pltpu.VMEM((HEAD_A, HEAD_A), F32)],
        compiler_params=_params(("parallel", "arbitrary")))(q, k, v, gb, tm_all, s_all, do)


def _onorm_fwd(o, gate, g, name):
    t = o.shape[0]

    def body(o_ref, gate_ref, g_ref, y_ref):
        ov, gv = o_ref[...], gate_ref[...]
        r = lax.rsqrt(jnp.mean(ov * ov, axis=-1, keepdims=True) + RMS_EPS)
        y_ref[...] = (ov * r * g_ref[...] * gv * _sigmoid(gv)).astype(BF16)

    blk = pl.BlockSpec((t, HEAD_A), lambda j: (0, j))
    return pl.pallas_call(
        body, name=name, grid=(N_HEADS_A,), in_specs=[blk, blk, pl.BlockSpec((1, HEAD_A), lambda j: (0, 0))],
        out_specs=blk, out_shape=jax.ShapeDtypeStruct((t, WIDTH_A), BF16),
        compiler_params=_params(("parallel",)))(o, gate, g)


def _onorm_bwd(o, gate, g, dy, name):
    t = o.shape[0]

    def body(o_ref, gate_ref, g_ref, dy_ref, do_ref, dgate_ref, dg_ref):
        @pl.when(pl.program_id(0) == 0)
        def _():
            dg_ref[...] = jnp.zeros_like(dg_ref)

        ov, gv, dyv = o_ref[...], gate_ref[...], dy_ref[...].astype(F32)
        r = lax.rsqrt(jnp.mean(ov * ov, axis=-1, keepdims=True) + RMS_EPS)
        oh = ov * r
        sg, dsg = _silu_and_grad(gv)
        dgate_ref[...] = (dyv * oh * g_ref[...] * dsg).astype(BF16)
        dn = dyv * sg
        dg_ref[...] += jnp.sum(dn * oh, axis=0, keepdims=True)
        dng = dn * g_ref[...]
        do_ref[...] = r * (dng - oh * jnp.mean(dng * oh, axis=-1, keepdims=True))

    blk = pl.BlockSpec((t, HEAD_A), lambda j: (0, j))
    vec = pl.BlockSpec((1, HEAD_A), lambda j: (0, 0))
    return pl.pallas_call(
        body, name=name, grid=(N_HEADS_A,), in_specs=[blk, blk, vec, blk], out_specs=[blk, blk, vec],
        out_shape=[jax.ShapeDtypeStruct((t, WIDTH_A), F32), jax.ShapeDtypeStruct((t, WIDTH_A), BF16),
                   jax.ShapeDtypeStruct((1, HEAD_A), F32)],
        compiler_params=_params(("arbitrary",)))(o, gate, g, dy)


def _cmul(ar, ai, br, bi):
    return ar * br - ai * bi, ar * bi + ai * br


def _scan_tables(ar, ai, reverse):
    p1 = (ar, ai)
    p2 = _cmul(*p1, *p1)
    p4 = _cmul(*p2, *p2)
    p8 = _cmul(*p4, *p4)
    p3 = _cmul(*p2, *p1)
    p5 = _cmul(*p4, *p1)
    p6 = _cmul(*p4, *p2)
    p7 = _cmul(*p4, *p3)
    pows = [p1, p2, p3, p4, p5, p6, p7, p8]
    rows = lax.broadcasted_iota(jnp.int32, (8, ar.shape[1]), 0)
    tr = jnp.zeros((8, ar.shape[1]), F32)
    ti = jnp.zeros((8, ar.shape[1]), F32)
    for r in range(8):
        pw = pows[7 - r] if reverse else pows[r]
        tr = jnp.where(rows == r, pw[0], tr)
        ti = jnp.where(rows == r, pw[1], ti)
    return p1, p2, p4, p8, tr, ti


def _tile_scan(xr, xi, p1, p2, p4, reverse):
    rows = lax.broadcasted_iota(jnp.int32, xr.shape, 0)
    for s, (pr, pi) in ((1, p1), (2, p2), (4, p4)):
        if reverse:
            keep = rows < 8 - s
            sr, si = pltpu.roll(xr, 8 - s, 0), pltpu.roll(xi, 8 - s, 0)
        else:
            keep = rows >= s
            sr, si = pltpu.roll(xr, s, 0), pltpu.roll(xi, s, 0)
        sr, si = jnp.where(keep, sr, 0.0), jnp.where(keep, si, 0.0)
        mr, mi = _cmul(pr, pi, sr, si)
        xr, xi = xr + mr, xi + mi
    return xr, xi


def _s5_scan_fwd(bu, a, name, tb=512, comm=None):
    t = bu.shape[0]
    cb = SCAN_CB
    nt = t // tb

    def body(b_ref, a_ref, x_ref, carry):
        @pl.when(pl.program_id(1) == 0)
        def _():
            carry[...] = jnp.zeros_like(carry)

        ar, ai = a_ref[:, 0:cb], a_ref[:, cb:2 * cb]
        p1, p2, p4, p8, tr, ti = _scan_tables(ar, ai, False)

        def step(j, c):
            cr, ci = c
            i = pl.multiple_of(j * 8, 8)
            xr, xi = _tile_scan(b_ref[pl.ds(i, 8), 0:cb], b_ref[pl.ds(i, 8), cb:2 * cb], p1, p2, p4, False)
            mr, mi = _cmul(tr, ti, cr, ci)
            xr, xi = xr + mr, xi + mi
            x_ref[pl.ds(i, 8), 0:cb] = xr
            x_ref[pl.ds(i, 8), cb:2 * cb] = xi
            return xr[7:8, :], xi[7:8, :]

        cr, ci = lax.fori_loop(0, tb // 8, step, (carry[0:1, :], carry[1:2, :]), unroll=2)
        carry[0:1, :] = cr
        carry[1:2, :] = ci

    blk = pl.BlockSpec((tb, 2 * cb), lambda j, i: (i, j))
    return _call(
        body, name=name, grid=(SSM_CH // cb, nt),
        in_specs=[blk, pl.BlockSpec((1, 2 * cb), lambda j, i: (0, j))], out_specs=blk,
        out_shape=jax.ShapeDtypeStruct((t, 2 * SSM_CH), F32), scratch_shapes=[pltpu.VMEM((8, cb), F32)],
        sem=("parallel", "arbitrary"), args=(bu, a), comm=comm)


def _s5_scan_bwd(dx, x, a, name, tb=512):
    t = dx.shape[0]
    cb = SCAN_CB
    nt = t // tb
    nj = tb // 8

    def body(d_ref, x_ref, xp_ref, a_ref, l_ref, da_ref, carry, acc):
        tblk = pl.program_id(1)

        @pl.when(tblk == 0)
        def _():
            carry[...] = jnp.zeros_like(carry)
            acc[...] = jnp.zeros_like(acc)

        ar, ai = a_ref[:, 0:cb], a_ref[:, cb:2 * cb]
        p1, p2, p4, p8, tr, ti = _scan_tables(ar, -ai, True)
        rows = lax.broadcasted_iota(jnp.int32, (8, cb), 0)

        def step(jj, c):
            cr, ci, sr_acc, si_acc = c
            j = nj - 1 - jj
            i = pl.multiple_of(j * 8, 8)
            lr, li = _tile_scan(d_ref[pl.ds(i, 8), 0:cb], d_ref[pl.ds(i, 8), cb:2 * cb], p1, p2, p4, True)
            mr, mi = _cmul(tr, ti, cr, ci)
            lr, li = lr + mr, li + mi
            l_ref[pl.ds(i, 8), 0:cb] = lr
            l_ref[pl.ds(i, 8), cb:2 * cb] = li
            ip = pl.multiple_of(jnp.maximum(j - 1, 0) * 8, 8)
            prev_r = jnp.where(j > 0, x_ref[pl.ds(ip, 8), 0:cb], xp_ref[:, 0:cb])
            prev_i = jnp.where(j > 0, x_ref[pl.ds(ip, 8), cb:2 * cb], xp_ref[:, cb:2 * cb])
            edge = jnp.where(jnp.logical_and(j == 0, tblk == nt - 1), 0.0, 1.0)
            xs_r = jnp.where(rows == 0, pltpu.roll(prev_r, 1, 0) * edge, pltpu.roll(x_ref[pl.ds(i, 8), 0:cb], 1, 0))
            xs_i = jnp.where(rows == 0, pltpu.roll(prev_i, 1, 0) * edge, pltpu.roll(x_ref[pl.ds(i, 8), cb:2 * cb], 1, 0))
            sr_acc = sr_acc + lr * xs_r + li * xs_i
            si_acc = si_acc + li * xs_r - lr * xs_i
            return lr[0:1, :], li[0:1, :], sr_acc, si_acc

        cr, ci, sr_acc, si_acc = lax.fori_loop(
            0, nj, step, (carry[0:1, :], carry[1:2, :], acc[:, 0:cb], acc[:, cb:2 * cb]))
        carry[0:1, :] = cr
        carry[1:2, :] = ci
        acc[:, 0:cb] = sr_acc
        acc[:, cb:2 * cb] = si_acc

        @pl.when(tblk == nt - 1)
        def _():
            da_ref[...] = jnp.sum(acc[...], axis=0, keepdims=True)

    blk = pl.BlockSpec((tb, 2 * cb), lambda j, i: (nt - 1 - i, j))
    prev = pl.BlockSpec((8, 2 * cb), lambda j, i: (jnp.maximum((nt - 1 - i) * (tb // 8) - 1, 0), j))
    vec = pl.BlockSpec((1, 2 * cb), lambda j, i: (0, j))
    return pl.pallas_call(
        body, name=name, grid=(SSM_CH // cb, nt), in_specs=[blk, blk, prev, vec], out_specs=[blk, vec],
        out_shape=[jax.ShapeDtypeStruct((t, 2 * SSM_CH), F32), jax.ShapeDtypeStruct((1, 2 * SSM_CH), F32)],
        scratch_shapes=[pltpu.VMEM((8, cb), F32), pltpu.VMEM((8, 2 * cb), F32)],
        compiler_params=_params(("parallel", "arbitrary")))(dx, x, x, a)


def _glu_fwd(yc, u, dvec, wg, bg, name, tr=256):
    t = yc.shape[0]

    def body(yc_ref, u_ref, d_ref, w_ref, b_ref, yl_ref, yb_ref):
        yl = yc_ref[...] + d_ref[...] * u_ref[...]
        yl_ref[...] = yl
        yg, _ = _gelu_and_grad(yl)
        z = jnp.dot(yg.astype(BF16), w_ref[...], preferred_element_type=F32) + b_ref[...]
        yb_ref[...] = (yg * _sigmoid(z)).astype(BF16)

    blk = pl.BlockSpec((tr, SSM_WIDTH), lambda i: (i, 0))
    vec = pl.BlockSpec((1, SSM_WIDTH), lambda i: (0, 0))
    return pl.pallas_call(
        body, name=name, grid=(t // tr,),
        in_specs=[blk, blk, vec, pl.BlockSpec((SSM_WIDTH, SSM_WIDTH), lambda i: (0, 0)), vec],
        out_specs=[blk, blk],
        out_shape=[jax.ShapeDtypeStruct((t, SSM_WIDTH), F32), jax.ShapeDtypeStruct((t, SSM_WIDTH), BF16)],
        compiler_params=_params(("parallel",)))(yc, u, dvec, wg, bg)


def _glu_bwd(yl, u, dvec, wg, bg, dyb, name, tr=256):
    t = yl.shape[0]

    def body(yl_ref, u_ref, d_ref, w_ref, b_ref, dy_ref, dyl_ref, du_ref, dw_ref, db_ref, dd_ref):
        @pl.when(pl.program_id(0) == 0)
        def _():
            dw_ref[...] = jnp.zeros_like(dw_ref)
            db_ref[...] = jnp.zeros_like(db_ref)
            dd_ref[...] = jnp.zeros_like(dd_ref)

        ylv, dyv, wv = yl_ref[...], dy_ref[...].astype(F32), w_ref[...]
        yg, dgelu = _gelu_and_grad(ylv)
        ygb = yg.astype(BF16)
        z = jnp.dot(ygb, wv, preferred_element_type=F32) + b_ref[...]
        sg = _sigmoid(z)
        dz = dyv * yg * sg * (1.0 - sg)
        dzb = dz.astype(BF16)
        dyg = dyv * sg + lax.dot_general(dzb, wv, (((1,), (1,)), ((), ())), preferred_element_type=F32)
        dyl = dyg * dgelu
        dyl_ref[...] = dyl.astype(BF16)
        du_ref[...] = dyl * d_ref[...]
        dw_ref[...] += lax.dot_general(ygb, dzb, (((0,), (0,)), ((), ())), preferred_element_type=F32)
        db_ref[...] += jnp.sum(dz, axis=0, keepdims=True)
        dd_ref[...] += jnp.sum(dyl * u_ref[...], axis=0, keepdims=True)

    blk = pl.BlockSpec((tr, SSM_WIDTH), lambda i: (i, 0))
    vec = pl.BlockSpec((1, SSM_WIDTH), lambda i: (0, 0))
    wsp = pl.BlockSpec((SSM_WIDTH, SSM_WIDTH), lambda i: (0, 0))
    return pl.pallas_call(
        body, name=name, grid=(t // tr,), in_specs=[blk, blk, vec, wsp, vec, blk],
        out_specs=[blk, blk, wsp, vec, vec],
        out_shape=[jax.ShapeDtypeStruct((t, SSM_WIDTH), BF16), jax.ShapeDtypeStruct((t, SSM_WIDTH), F32),
                   jax.ShapeDtypeStruct((SSM_WIDTH, SSM_WIDTH), F32), jax.ShapeDtypeStruct((1, SSM_WIDTH), F32),
                   jax.ShapeDtypeStruct((1, SSM_WIDTH), F32)],
        compiler_params=_params(("arbitrary",)))(yl, u, dvec, wg, bg, dyb)


def _mesh_pos():
    return lax.axis_index("x"), lax.axis_index("y"), lax.axis_index("c")


def _device_index():
    x, y, c = _mesh_pos()
    return 4 * x + 2 * y + c


def _gather_comm(arrays):
    na = len(arrays)

    def ctx(ins, outs, sems):
        send_sems, recv_sems = sems
        x, y, c = _mesh_pos()
        chips = [(1 - x, y), (x, 1 - y), (1 - x, 1 - y)]

        def copy(ai, kk, block, to, own=False):
            slot = outs[ai].at[4 * block[0] + 2 * block[1] + block[2]]
            return pltpu.make_async_remote_copy(
                src_ref=ins[ai] if own else slot, dst_ref=slot, send_sem=send_sems.at[ai, kk],
                recv_sem=recv_sems.at[ai, kk], device_id=to, device_id_type=MESH)

        return (x, y, c), (x, y, 1 - c), chips, c, copy

    def start(ins, outs, sems):
        me, sibling, chips, c, copy = ctx(ins, outs, sems)
        for ai in range(na):
            copy(ai, 0, me, sibling, own=True).start()
            for j, chip in enumerate(chips):
                copy(ai, 1 + j, me, (*chip, c), own=True).start()

    def mid(ins, outs, sems):
        me, sibling, chips, c, copy = ctx(ins, outs, sems)
        for ai in range(na):
            for j, chip in enumerate(chips):
                copy(ai, 1 + j, (*chip, c), me).wait_recv()
                copy(ai, 4 + j, (*chip, c), sibling).start()

    def end(ins, outs, sems):
        me, sibling, chips, c, copy = ctx(ins, outs, sems)
        for ai in range(na):
            copy(ai, 0, sibling, me).wait_recv()
            copy(ai, 0, me, sibling, own=True).wait_send()
            for j, chip in enumerate(chips):
                copy(ai, 4 + j, (*chip, 1 - c), me).wait_recv()
                copy(ai, 1 + j, me, (*chip, c), own=True).wait_send()
                copy(ai, 4 + j, (*chip, c), sibling).wait_send()

    return Comm(arrays, [jax.ShapeDtypeStruct((N_DEV,) + a.shape, a.dtype) for a in arrays],
                [pltpu.SemaphoreType.DMA((na, 7)), pltpu.SemaphoreType.DMA((na, 7))], start, end, mid)


def _fill_own(gathered, local):
    return lax.dynamic_update_index_in_dim(gathered, local, _device_index(), 0)


def _swap_comm(arrays):
    na = len(arrays)
    offs = np.concatenate([[0], np.cumsum([a.shape[1] for a in arrays])]).astype(int)

    def copies(ins, outs, sems):
        x, y, c = _mesh_pos()
        return [pltpu.make_async_remote_copy(
            src_ref=ins[ai].at[2 * k + 1 - c], dst_ref=outs[0].at[k, pl.ds(int(offs[ai]), arrays[ai].shape[1])],
            send_sem=sems[0].at[ai, k], recv_sem=sems[1].at[ai, k], device_id=(x, y, 1 - c), device_id_type=MESH)
            for ai in range(na) for k in range(4)]

    def start(ins, outs, sems):
        for cp in copies(ins, outs, sems):
            cp.start()

    def end(ins, outs, sems):
        for cp in copies(ins, outs, sems):
            cp.wait()

    return Comm(arrays, [jax.ShapeDtypeStruct((4, int(offs[-1]), PACK_COLS), arrays[0].dtype)],
                [pltpu.SemaphoreType.DMA((na, 4)), pltpu.SemaphoreType.DMA((na, 4))], start, end)


def _chips_comm(send):
    def copies(ins, outs, sems):
        x, y, c = _mesh_pos()
        chips = [(1 - x, y), (x, 1 - y), (1 - x, 1 - y)]
        return [pltpu.make_async_remote_copy(
            src_ref=ins[0].at[2 * cx + cy], dst_ref=outs[0].at[j], send_sem=sems[0].at[j], recv_sem=sems[1].at[j],
            device_id=(cx, cy, c), device_id_type=MESH) for j, (cx, cy) in enumerate(chips)]

    def start(ins, outs, sems):
        for cp in copies(ins, outs, sems):
            cp.start()

    def end(ins, outs, sems):
        for cp in copies(ins, outs, sems):
            cp.wait()

    return Comm([send], [jax.ShapeDtypeStruct((3,) + send.shape[1:], send.dtype)],
                [pltpu.SemaphoreType.DMA((3,)), pltpu.SemaphoreType.DMA((3,))], start, end)


def _all_gather(arrays, name):
    na = len(arrays)

    def body(*refs):
        ins, outs = refs[:na], refs[na:2 * na]
        send_sems, recv_sems, local_sems = refs[2 * na:]
        x, y, c = _mesh_pos()
        me, sibling = (x, y, c), (x, y, 1 - c)
        chips = [(1 - x, y), (x, 1 - y), (1 - x, 1 - y)]
        waits = []
        for ai in range(na):
            in_ref, out_ref = ins[ai], outs[ai]

            def slot(px, py, pc, out_ref=out_ref):
                return out_ref.at[4 * px + 2 * py + pc]

            def copy(kk, block, to, src=None, ai=ai, slot=slot):
                return pltpu.make_async_remote_copy(
                    src_ref=slot(*block) if src is None else src, dst_ref=slot(*block),
                    send_sem=send_sems.at[ai, kk], recv_sem=recv_sems.at[ai, kk], device_id=to, device_id_type=MESH)

            mine = pltpu.make_async_copy(in_ref, slot(*me), local_sems.at[ai])
            mine.start()
            first = [copy(0, me, sibling, src=in_ref)]
            first += [copy(1 + j, me, (*chip, c), src=in_ref) for j, chip in enumerate(chips)]
            for cp in first:
                cp.start()
            waits.append((copy, mine, first))
        sends = []
        for ai in range(na):
            copy, mine, first = waits[ai]
            passed = [copy(4 + j, (*chip, c), sibling) for j, chip in enumerate(chips)]
            for j, chip in enumerate(chips):
                copy(1 + j, (*chip, c), me).wait_recv()
                passed[j].start()
            sends.append(passed)
        for ai in range(na):
            copy, mine, first = waits[ai]
            copy(0, sibling, me).wait_recv()
            for j, chip in enumerate(chips):
                copy(4 + j, (*chip, 1 - c), me).wait_recv()
            for cp in first + sends[ai]:
                cp.wait_send()
            mine.wait()

    any_spec = pl.BlockSpec(memory_space=pl.ANY)
    return pl.pallas_call(
        body, name=name, in_specs=[any_spec] * na, out_specs=[any_spec] * na,
        out_shape=[jax.ShapeDtypeStruct((N_DEV,) + a.shape, a.dtype) for a in arrays],
        scratch_shapes=[pltpu.SemaphoreType.DMA((na, 7)), pltpu.SemaphoreType.DMA((na, 7)),
                        pltpu.SemaphoreType.DMA((na,))],
        compiler_params=pltpu.CompilerParams(has_side_effects=True))(*arrays)


def _swap_sibling(arrays, name):
    na = len(arrays)
    offs = np.concatenate([[0], np.cumsum([a.shape[1] for a in arrays])]).astype(int)
    rows = int(offs[-1])

    def body(*refs):
        ins, recv_ref = refs[:na], refs[na]
        send_sems, recv_sems = refs[na + 1:]
        x, y, c = _mesh_pos()
        started = []
        for ai in range(na):
            span = pl.ds(int(offs[ai]), arrays[ai].shape[1])
            for k in range(4):
                remote = pltpu.make_async_remote_copy(
                    src_ref=ins[ai].at[2 * k + 1 - c], dst_ref=recv_ref.at[k, span], send_sem=send_sems.at[ai, k],
                    recv_sem=recv_sems.at[ai, k], device_id=(x, y, 1 - c), device_id_type=MESH)
                remote.start()
                started.append(remote)
        for remote in started:
            remote.wait()

    any_spec = pl.BlockSpec(memory_space=pl.ANY)
    return pl.pallas_call(
        body, name=name, in_specs=[any_spec] * na, out_specs=any_spec,
        out_shape=jax.ShapeDtypeStruct((4, rows, PACK_COLS), arrays[0].dtype),
        scratch_shapes=[pltpu.SemaphoreType.DMA((na, 4)), pltpu.SemaphoreType.DMA((na, 4))])(*arrays)


def _exchange_chips(send, name):
    def body(s_ref, o_ref, send_sems, recv_sems):
        x, y, c = _mesh_pos()
        chips = [(1 - x, y), (x, 1 - y), (1 - x, 1 - y)]
        cps = [pltpu.make_async_remote_copy(
            src_ref=s_ref.at[2 * cx + cy], dst_ref=o_ref.at[j], send_sem=send_sems.at[j], recv_sem=recv_sems.at[j],
            device_id=(cx, cy, c), device_id_type=MESH) for j, (cx, cy) in enumerate(chips)]
        for cp in cps:
            cp.start()
        for cp in cps:
            cp.wait()

    any_spec = pl.BlockSpec(memory_space=pl.ANY)
    return pl.pallas_call(
        body, name=name, in_specs=[any_spec], out_specs=any_spec,
        out_shape=jax.ShapeDtypeStruct((3,) + send.shape[1:], send.dtype),
        scratch_shapes=[pltpu.SemaphoreType.DMA((3,)), pltpu.SemaphoreType.DMA((3,))])(send)


def _pair_sum(keep, recv, name, tr=464):
    nchip, rows, cols = keep.shape

    def body(g_ref, r_ref, o_ref):
        o_ref[...] = (g_ref[...].astype(F32) + r_ref[...].astype(F32)).astype(BF16)

    blk = pl.BlockSpec((1, tr, cols), lambda k, i: (k, i, 0))
    return pl.pallas_call(
        body, name=name, grid=(nchip, rows // tr), in_specs=[blk, blk], out_specs=blk,
        out_shape=jax.ShapeDtypeStruct((nchip, rows, cols), BF16),
        compiler_params=_params(("parallel", "parallel")))(keep, recv)


def _chip_sum(own, others, name, tr=464):
    _, rows, cols = own.shape
    chip = (2 * lax.axis_index("x") + lax.axis_index("y")).astype(jnp.int32).reshape(1)

    def body(chip_ref, own_ref, oth_ref, o_ref):
        del chip_ref
        acc = own_ref[0].astype(F32)
        for j in range(3):
            acc = acc + oth_ref[j].astype(F32)
        o_ref[...] = acc

    grid_spec = pltpu.PrefetchScalarGridSpec(
        num_scalar_prefetch=1, grid=(rows // tr,),
        in_specs=[pl.BlockSpec((1, tr, cols), lambda i, chip_ref: (chip_ref[0], i, 0)),
                  pl.BlockSpec((3, tr, cols), lambda i, chip_ref: (0, i, 0))],
        out_specs=pl.BlockSpec((tr, cols), lambda i, chip_ref: (i, 0)))
    return pl.pallas_call(
        body, name=name, grid_spec=grid_spec, out_shape=jax.ShapeDtypeStruct((rows, cols), F32),
        compiler_params=_params(("parallel",)))(chip, own, others)


def _sum_leading(parts, name, tr=464):
    nparts, rows, cols = parts.shape
    tr = tr if rows % tr == 0 else rows

    def body(p_ref, o_ref):
        acc = p_ref[0].astype(F32)
        for i in range(1, nparts):
            acc = acc + p_ref[i].astype(F32)
        o_ref[...] = acc

    return pl.pallas_call(
        body, name=name, grid=(rows // tr,),
        in_specs=[pl.BlockSpec((nparts, tr, cols), lambda i: (0, i, 0))],
        out_specs=pl.BlockSpec((tr, cols), lambda i: (i, 0)), out_shape=jax.ShapeDtypeStruct((rows, cols), F32),
        compiler_params=_params(("parallel",)))(parts)


def _adamw(w, g, m, v, name):
    shape = w.shape
    cols = shape[-1]
    rows = int(np.prod(shape[:-1])) if len(shape) > 1 else 1
    w2, g2, m2, v2 = (a.reshape(rows, cols) for a in (w, g, m, v))
    tr = rows
    for cand in (512, 256, 128, 64, 32, 16, 8):
        if rows % cand == 0 and rows > cand:
            tr = cand
            break
    bc1, bc2 = 1.0 - ADAM_B1 ** ADAM_STEP, 1.0 - ADAM_B2 ** ADAM_STEP

    def body(w_ref, g_ref, m_ref, v_ref, d_ref, nm_ref, nv_ref):
        gv = g_ref[...]
        nm = ADAM_B1 * m_ref[...] + (1.0 - ADAM_B1) * gv
        nv = ADAM_B2 * v_ref[...] + (1.0 - ADAM_B2) * (gv * gv)
        nm_ref[...] = nm
        nv_ref[...] = nv
        d_ref[...] = -ADAM_LR * ((nm / bc1) / (jnp.sqrt(nv / bc2) + ADAM_EPS) + ADAM_WD * w_ref[...])

    blk = pl.BlockSpec((tr, cols), lambda i: (i, 0))
    outs = pl.pallas_call(
        body, name=name, grid=(rows // tr,), in_specs=[blk] * 4, out_specs=[blk] * 3,
        out_shape=[jax.ShapeDtypeStruct((rows, cols), F32)] * 3, compiler_params=_params(("parallel",)))(w2, g2, m2, v2)
    return tuple(o.reshape(shape) for o in outs)


WEIGHT_NAMES = ['norm_mix_g', 'norm_xa_g', 'norm_ffn_g', 'norm_mem_g', 'norm_final_g', 'w_in_ab', 'conv_qkv_a',
                'a_log_a', 'dt_bias_a', 'onorm_g_a', 'ssm_lambda_re', 'ssm_lambda_im', 'ssm_b_re', 'ssm_b_im',
                'ssm_c_re', 'ssm_c_im', 'ssm_d', 'ssm_log_dt', 'w_glu_b', 'b_glu_b', 'w_out_ab', 'pool_w',
                'pool_scale', 'xa_wq', 'xa_wkv', 'xa_wo', 'ffn_w_up', 'ffn_conv', 'ffn_w_down']
BIG_SHARDED = {'w_in_ab': ((1, 1024, 2568), 2), 'w_glu_b': ((1, 512, 512), 1), 'w_out_ab': ((1, 1024, 1024), 1),
               'pool_w': ((1, 4, 256, 256), 2), 'xa_wq': ((2, 1024, 1024), 1), 'xa_wkv': ((2, 1024, 2048), 2),
               'xa_wo': ((2, 1024, 1024), 1), 'ffn_w_up': ((2, 1024, 5632), 2), 'ffn_w_down': ((2, 2816, 1024), 1)}
SMALL_SHARDED = {'conv_qkv_a': ((1, 4, 1536), 2), 'pool_scale': ((1, 1024), 1), 'ffn_conv': ((2, 3, 5632), 2)}
REPLICATED = {'norm_mix_g': (2, 1024), 'norm_xa_g': (2, 1024), 'norm_ffn_g': (2, 1024), 'norm_mem_g': (1024,),
              'norm_final_g': (1024,), 'a_log_a': (1, 4), 'dt_bias_a': (1, 4), 'onorm_g_a': (1, 128),
              'ssm_lambda_re': (1, 32, 64), 'ssm_lambda_im': (1, 32, 64), 'ssm_b_re': (1, 32, 64, 16),
              'ssm_b_im': (1, 32, 64, 16), 'ssm_c_re': (1, 32, 16, 64), 'ssm_c_im': (1, 32, 16, 64),
              'ssm_d': (1, 32, 16), 'ssm_log_dt': (1, 32), 'b_glu_b': (1, 512)}
BIG_ROW_ALIGN = 464 * 8


def _shard_shape(shape, axis):
    return tuple(s // N_DEV if i == axis else s for i, s in enumerate(shape))


def _pad_rows(a, rows):
    return a if a.shape[0] == rows else jnp.concatenate([a, jnp.zeros((rows - a.shape[0],) + a.shape[1:], a.dtype)], 0)


def _round_up(n, m):
    return (n + m - 1) // m * m


def _pack_rows(flat_list, align):
    rows = []
    for a in flat_list:
        n = a.shape[0]
        r = _round_up(n, PACK_COLS) // PACK_COLS
        if n != r * PACK_COLS:
            a = jnp.concatenate([a, jnp.zeros((r * PACK_COLS - n,), a.dtype)])
        rows.append(a.reshape(r, PACK_COLS))
    out = jnp.concatenate(rows, 0)
    return _pad_rows(out, _round_up(out.shape[0], align))


def _row_offsets(sizes):
    offs, r = [], 0
    for n in sizes:
        offs.append(r)
        r += _round_up(n, PACK_COLS) // PACK_COLS
    return offs


def _split_shards(full, axis):
    shape = full.shape
    s = shape[axis] // N_DEV
    a = full.reshape(shape[:axis] + (N_DEV, s) + shape[axis + 1:])
    return jnp.moveaxis(a, axis, 0).reshape(N_DEV, -1)


def _merge_shards(pieces, shape, axis):
    sh = _shard_shape(shape, axis)
    a = pieces.reshape((N_DEV,) + sh)
    a = jnp.moveaxis(a, 0, axis)
    return a.reshape(shape)


_SCAN_NB = SSM_CH // SCAN_CB


def _to_scan_layout(m, axis):
    shape = m.shape
    m = m.reshape(shape[:axis] + (2, _SCAN_NB, SCAN_CB) + shape[axis + 1:])
    return jnp.swapaxes(m, axis, axis + 1).reshape(shape)


def _from_scan_layout(m, axis):
    shape = m.shape
    m = m.reshape(shape[:axis] + (_SCAN_NB, 2, SCAN_CB) + shape[axis + 1:])
    return jnp.swapaxes(m, axis, axis + 1).reshape(shape)


def _s5_discretise(lam_re, lam_im, b_re, b_im, log_dt):
    dt = jnp.exp(log_dt)[:, None]
    mag = jnp.exp(lam_re * dt)
    ang = lam_im * dt
    lb_re, lb_im = mag * jnp.cos(ang), mag * jnp.sin(ang)
    den = lam_re * lam_re + lam_im * lam_im
    nr, ni = lb_re - 1.0, lb_im
    coef_re = (nr * lam_re + ni * lam_im) / den
    coef_im = (ni * lam_re - nr * lam_im) / den
    bb_re = coef_re[..., None] * b_re - coef_im[..., None] * b_im
    bb_im = coef_re[..., None] * b_im + coef_im[..., None] * b_re
    return lb_re, lb_im, bb_re, bb_im


def _s5_matrices(lb_re, lb_im, bb_re, bb_im, c_re, c_im):
    eye = jnp.eye(N_GROUPS, dtype=F32)
    bmat = lambda bb: jnp.einsum('gph,gk->ghkp', bb, eye).reshape(SSM_WIDTH, SSM_CH)
    cmat = lambda cc: jnp.einsum('ghp,gk->gpkh', cc, eye).reshape(SSM_CH, SSM_WIDTH)
    b_in = _to_scan_layout(jnp.concatenate([bmat(bb_re), bmat(bb_im)], axis=1), 1)
    c_out = _to_scan_layout(jnp.concatenate([cmat(c_re), -cmat(c_im)], axis=0), 0)
    a_row = _to_scan_layout(jnp.concatenate([lb_re.reshape(1, SSM_CH), lb_im.reshape(1, SSM_CH)], axis=1), 1)
    return b_in, c_out, a_row


def _s5_matrix_grads(db_in, dc_out, da_row):
    db_nat = _from_scan_layout(db_in, 1)
    dc_nat = _from_scan_layout(dc_out, 0)
    da_nat = _from_scan_layout(da_row, 1)
    eye = jnp.eye(N_GROUPS, dtype=F32)
    bgrad = lambda m: jnp.einsum('ghkp,gk->gph', m.reshape(N_GROUPS, SSM_GROUP, N_GROUPS, SSM_STATE), eye)
    cgrad = lambda m: jnp.einsum('gpkh,gk->ghp', m.reshape(N_GROUPS, SSM_STATE, N_GROUPS, SSM_GROUP), eye)
    dbb_re, dbb_im = bgrad(db_nat[:, :SSM_CH]), bgrad(db_nat[:, SSM_CH:])
    dc_re, dc_im = cgrad(dc_nat[:SSM_CH]), -cgrad(dc_nat[SSM_CH:])
    dlb_re = da_nat[0, :SSM_CH].reshape(N_GROUPS, SSM_STATE)
    dlb_im = da_nat[0, SSM_CH:].reshape(N_GROUPS, SSM_STATE)
    return dlb_re, dlb_im, dbb_re, dbb_im, dc_re, dc_im


def _hybrid_fwd(xn, x, wts, p, weights):
    sv = {}
    hq = _mm(xn, wts['w_qkv_t'], "nt", "l0_in_qkv")
    gate = _mm(xn, wts['w_gate_t'], "nt", "l0_in_gate")
    ba = _mm(xn, wts['w_ba_t'], "nt", "l0_in_ba")
    u = _mm(xn, wts['w_u_t'], "nt", "l0_in_u")
    conv = p['conv_qkv']
    q = _qkv_pre_fwd(hq, conv, 0, 4, True, HEAD_A ** -0.5, "l0_q_pre")
    k = _qkv_pre_fwd(hq, conv, 4, 4, True, 1.0, "l0_k_pre")
    v = _qkv_pre_fwd(hq, conv, 8, 4, False, 1.0, "l0_v_pre")
    gates = _gates_fwd(ba, p['arow'], p['brow'], "l0_gates")
    gb = jnp.stack([gates[:, 0:4].T, gates[:, 4:8].T], axis=-1)
    (o, tm_all, s_all), (got,) = _gdr_fwd(q, k, v, gb, "l0_gdr_fwd", comm=weights.plan(GATHER_ON_GDR))
    weights.land(GATHER_ON_GDR, got)
    wts['w_glu'], wts['w_out'] = weights.full['w_glu'], weights.full['w_out']
    y_a = _onorm_fwd(o, gate, p['onorm_g'], "l0_onorm")
    bu = _mm(u, p['b_in'], "nn", "l0_s5_bu")
    xs, (got,) = _s5_scan_fwd(bu, p['a_row'], "l0_s5_scan", comm=weights.plan(GATHER_ON_SCAN))
    weights.land(GATHER_ON_SCAN, got)
    yc = _mm(xs, p['c_out'], "nn", "l0_s5_cx")
    yl, y_b = _glu_fwd(yc, u, p['d_row'], wts['w_glu'], p['b_glu'], "l0_glu")
    mixed = jnp.concatenate([y_a, y_b], axis=1)
    x1 = _mm(mixed, wts['w_out'], "nn", "l0_out", res=x)
    sv.update(hq=hq, gate=gate, ba=ba, u=u, q=q, k=k, v=v, gb=gb, o=o, tm=tm_all, s=s_all, xs=xs, yl=yl, mixed=mixed)
    return x1, sv


def _hybrid_bwd(dx1, xn, wts, p, sv):
    gr = {}
    dmixed = _mm(dx1, wts['w_out'], "nt", "l0_out_dx", out_dtype=BF16)
    gr['w_out_ab'] = _mm(sv['mixed'], dx1, "tn", "l0_out_dw", out_dtype=BF16)
    dya, dyb = dmixed[:, :WIDTH_A], dmixed[:, WIDTH_A:]
    dyl, du_direct, gr['w_glu_b'], gr['b_glu_b'], dd = _glu_bwd(
        sv['yl'], sv['u'], p['d_row'], wts['w_glu'], p['b_glu'], dyb, "l0_glu_bwd")
    dxs = _mm(dyl, p['c_out'], "nt", "l0_s5_cx_dx")
    dc_out = _mm(sv['xs'], dyl, "tn", "l0_s5_cx_dw")
    lam, da_row = _s5_scan_bwd(dxs, sv['xs'], p['a_row'], "l0_s5_scan_bwd")
    du = _mm(lam, p['b_in'], "nt", "l0_s5_bu_dx", res=du_direct, out_dtype=BF16)
    db_in = _mm(sv['u'], lam, "tn", "l0_s5_bu_dw")
    gr['s5'] = (db_in, dc_out, da_row, dd)
    do, dgate, gr['onorm_g_a'] = _onorm_bwd(sv['o'], sv['gate'], p['onorm_g'], dya, "l0_onorm_bwd")
    dq, dk, dv, dgb = _gdr_bwd(sv['q'], sv['k'], sv['v'], sv['gb'], sv['tm'], sv['s'], do, "l0_gdr_bwd")
    conv = p['conv_qkv']
    dhq_q, dcw_q = _qkv_pre_bwd(sv['hq'], conv, dq, 0, 4, True, HEAD_A ** -0.5, "l0_q_pre_bwd")
    dhq_k, dcw_k = _qkv_pre_bwd(sv['hq'], conv, dk, 4, 4, True, 1.0, "l0_k_pre_bwd")
    dhq_v, dcw_v = _qkv_pre_bwd(sv['hq'], conv, dv, 8, 4, False, 1.0, "l0_v_pre_bwd")
    gr['conv_qkv_a'] = jnp.concatenate([dcw_q, dcw_k, dcw_v], axis=1)
    dhq = jnp.concatenate([dhq_q, dhq_k, dhq_v], axis=1)
    dgates = jnp.concatenate([dgb[:, :, 0].T, dgb[:, :, 1].T, jnp.zeros((SEQ, LANE - 8), F32)], axis=1)
    dba, da_log, ddt_bias = _gates_bwd(sv['ba'], p['arow'], p['brow'], dgates, "l0_gates_bwd")
    gr['a_log_a'], gr['dt_bias_a'] = da_log[:, 4:8], ddt_bias[:, 4:8]
    dxn = _mm(dhq, wts['w_qkv_t'], "nn", "l0_in_qkv_dx")
    dxn = _mm(dgate, wts['w_gate_t'], "nn", "l0_in_gate_dx", res=dxn)
    dxn = _mm(dba, wts['w_ba_t'], "nn", "l0_in_ba_dx", res=dxn)
    dxn = _mm(du, wts['w_u_t'], "nn", "l0_in_u_dx", res=dxn)
    dw_qkv_t = _mm(dhq, xn, "tn", "l0_in_qkv_dw", out_dtype=BF16)
    dw_gate_t = _mm(dgate, xn, "tn", "l0_in_gate_dw", out_dtype=BF16)
    dw_ba_t = _mm(dba, xn, "tn", "l0_in_ba_dw", out_dtype=BF16)
    dw_u_t = _mm(du, xn, "tn", "l0_in_u_dw", out_dtype=BF16)
    gr['w_in_t'] = jnp.concatenate([dw_qkv_t, dw_gate_t, dw_ba_t[:8], dw_u_t], axis=0)
    return dxn, gr


def _xa_fwd(x1, g, mem_n, wq, wkv_t, wo, tag):
    xq = _rms_fwd(x1, g, BF16, tag + "_norm")
    q = _mm(xq, wq, "nn", tag + "_q", out_dtype=BF16)
    kv = _mm(mem_n, wkv_t, "nt", tag + "_kv", out_dtype=BF16)
    o = _attn_fwd(q, kv, tag + "_attn")
    x2 = _mm(o, wo, "nn", tag + "_o", res=x1)
    return x2, dict(xq=xq, q=q, kv=kv, o=o)


def _xa_bwd(dx2, x1, g, mem_n, wq, wkv_t, wo, sv, tag):
    do = _mm(dx2, wo, "nt", tag + "_o_dx", out_dtype=BF16)
    dwo = _mm(sv['o'], dx2, "tn", tag + "_o_dw", out_dtype=BF16)
    dq, dk, dv = _attn_bwd(sv['q'], sv['kv'], do, tag + "_attn_bwd")
    dkv = jnp.concatenate([dk, dv], axis=1).astype(BF16)
    dxq = _mm(dq, wq, "nt", tag + "_q_dx")
    dwq = _mm(sv['xq'], dq, "tn", tag + "_q_dw", out_dtype=BF16)
    dmem_n = _mm(dkv, wkv_t, "nn", tag + "_kv_dx")
    dwkv_t = _mm(dkv, mem_n, "tn", tag + "_kv_dw", out_dtype=BF16)
    dx1, dg = _rms_bwd(x1, g, dxq, dx2, tag + "_norm_bwd")
    return dx1, dmem_n, dict(wq=dwq, wkv_t=dwkv_t, wo=dwo, g=dg)


def _ffn_fwd(x2, g, w_up_t, conv, w_down, tag, weights=None, gather_on_up=(), gather_on_act=()):
    xf = _rms_fwd(x2, g, BF16, tag + "_norm")
    if weights is None:
        h = _mm(xf, w_up_t, "nt", tag + "_up")
        a = _ffn_act_fwd(h, conv, tag + "_act")
    else:
        h, (got,) = _mm(xf, w_up_t, "nt", tag + "_up", comm=weights.plan(gather_on_up))
        weights.land(gather_on_up, got)
        a, (got,) = _ffn_act_fwd(h, conv, tag + "_act", comm=weights.plan(gather_on_act))
        weights.land(gather_on_act, got)
    x3 = _mm(a, w_down, "nn", tag + "_down", res=x2)
    return x3, dict(xf=xf, h=h, a=a)


def _ffn_bwd(dx3, x2, g, w_up_t, conv, w_down, sv, tag):
    da = _mm(dx3, w_down, "nt", tag + "_down_dx")
    dw_down = _mm(sv['a'], dx3, "tn", tag + "_down_dw", out_dtype=BF16)
    dh, dconv = _ffn_act_bwd(sv['h'], conv, da, tag + "_act_bwd")
    dxf = _mm(dh, w_up_t, "nn", tag + "_up_dx")
    dw_up_t = _mm(dh, sv['xf'], "tn", tag + "_up_dw", out_dtype=BF16)
    dx2, dg = _rms_bwd(x2, g, dxf, dx3, tag + "_norm_bwd")
    return dx2, dict(w_up_t=dw_up_t, conv=dconv, w_down=dw_down, g=dg)


BIG_NAMES, SMALL_NAMES, REP_NAMES = list(BIG_SHARDED), list(SMALL_SHARDED), list(REPLICATED)
BIG_SIZES = [int(np.prod(_shard_shape(*BIG_SHARDED[n]))) for n in BIG_NAMES]
SMALL_SIZES = [int(np.prod(_shard_shape(*SMALL_SHARDED[n]))) for n in SMALL_NAMES]


PIECES = [('w_in_t', 384), ('w_glu', 32), ('w_out', 128), ('pool_w', 32), ('wq0', 128), ('wq1', 128),
          ('wkv_t0', 256), ('wkv_t1', 256), ('wo0', 128), ('wo1', 128), ('up_t0', 704), ('up_t1', 704),
          ('down0', 352), ('down1', 352)]
PIECE_OFFS = dict(zip([k for k, _ in PIECES], np.concatenate([[0], np.cumsum([r for _, r in PIECES])[:-1]]).tolist()))
W_IN_ROWS = 4 * WIDTH_A + 2 * N_HEADS_A + SSM_WIDTH
W_IN_PIECE = W_IN_ROWS // N_DEV


class _Weights:
    def __init__(self, inp):
        bf = lambda a: a.astype(BF16)
        local = {'w_in_t': bf(inp['w_in_ab'][0]).T, 'w_glu': bf(inp['w_glu_b'][0]), 'w_out': bf(inp['w_out_ab'][0]),
                 'pool_w': bf(inp['pool_w'][0]),
                 'small': _pack_rows([inp[n].reshape(-1) for n in SMALL_NAMES], 8)}
        for l in range(2):
            local['wq%d' % l] = bf(inp['xa_wq'][l])
            local['wkv_t%d' % l] = bf(inp['xa_wkv'][l]).T
            local['wo%d' % l] = bf(inp['xa_wo'][l])
            local['up_t%d' % l] = bf(inp['ffn_w_up'][l]).T
            local['down%d' % l] = bf(inp['ffn_w_down'][l])
        self.local, self.full = local, {}

    def plan(self, keys):
        return _gather_comm([self.local[k] for k in keys])

    def land(self, keys, gathered):
        for k, g in zip(keys, gathered):
            g = _fill_own(g, self.local[k])
            if k == 'small':
                for n, off, size in zip(SMALL_NAMES, _row_offsets(SMALL_SIZES), SMALL_SIZES):
                    r = _round_up(size, PACK_COLS) // PACK_COLS
                    self.full[n] = _merge_shards(g[:, off:off + r].reshape(N_DEV, -1)[:, :size], *SMALL_SHARDED[n])
            elif k == 'pool_w':
                self.full[k] = jnp.swapaxes(g, 0, 1).reshape(len(POOL_WINDOWS), POOL_GROUP, POOL_GROUP)
            else:
                self.full[k] = g.reshape(N_DEV * g.shape[1], g.shape[2])


GATHER_FIRST = ['w_in_t', 'small']
GATHER_ON_GDR = ['w_glu', 'w_out', 'wq0', 'wkv_t0', 'wo0', 'up_t0', 'down0']
GATHER_ON_SCAN = ['pool_w', 'wq1', 'wkv_t1', 'wo1']
GATHER_ON_FFN_UP = ['up_t1']
GATHER_ON_FFN_ACT = ['down1']


def _local_step(inp):
    f32_of = lambda n: inp[n].astype(F32)
    weights = _Weights(inp)
    full = weights.full
    weights.land(GATHER_FIRST, _comm_only(weights.plan(GATHER_FIRST), "gather_first"))
    w_in_t = full['w_in_t']
    wts0 = dict(w_qkv_t=w_in_t[:3 * WIDTH_A], w_gate_t=w_in_t[3 * WIDTH_A:4 * WIDTH_A],
                w_ba_t=jnp.concatenate([w_in_t[4 * WIDTH_A:4 * WIDTH_A + 8], jnp.zeros((LANE - 8, D_MODEL), BF16)], 0),
                w_u_t=w_in_t[4 * WIDTH_A + 8:])
    lb_disc, disc_vjp = jax.vjp(_s5_discretise, f32_of('ssm_lambda_re')[0], f32_of('ssm_lambda_im')[0],
                                f32_of('ssm_b_re')[0], f32_of('ssm_b_im')[0], f32_of('ssm_log_dt')[0])
    b_in, c_out, a_row = _s5_matrices(*lb_disc, f32_of('ssm_c_re')[0], f32_of('ssm_c_im')[0])
    zeros4 = jnp.zeros((1, 4), F32)
    p0 = dict(conv_qkv=full['conv_qkv_a'][0], onorm_g=f32_of('onorm_g_a'),
              arow=jnp.concatenate([zeros4, f32_of('a_log_a'), jnp.zeros((1, LANE - 8), F32)], 1),
              brow=jnp.concatenate([zeros4, f32_of('dt_bias_a'), jnp.zeros((1, LANE - 8), F32)], 1),
              b_in=b_in.astype(BF16), c_out=c_out.astype(BF16), a_row=a_row,
              d_row=f32_of('ssm_d').reshape(1, SSM_WIDTH), b_glu=f32_of('b_glu_b'))

    x0 = inp['x'][0]
    mem_n = _rms_fwd(inp['mem'][0], inp['norm_mem_g'], BF16, "mem_norm")
    xn0 = _rms_fwd(x0, inp['norm_mix_g'][0], BF16, "l0_mix_norm")
    x1, sv_mix0 = _hybrid_fwd(xn0, x0, wts0, p0, weights)
    x2, sv_xa0 = _xa_fwd(x1, inp['norm_xa_g'][0], mem_n, full['wq0'], full['wkv_t0'], full['wo0'], "l0_xa")
    x3, sv_ffn0 = _ffn_fwd(x2, inp['norm_ffn_g'][0], full['up_t0'], full['ffn_conv'][0], full['down0'], "l0_ffn",
                           weights, GATHER_ON_FFN_UP, GATHER_ON_FFN_ACT)
    xn1 = _rms_fwd(x3, inp['norm_mix_g'][1], F32, "l1_mix_norm")
    x4 = _pool_fwd(xn1, full['pool_w'], full['pool_scale'], x3, "l1_pool")
    x5, sv_xa1 = _xa_fwd(x4, inp['norm_xa_g'][1], mem_n, full['wq1'], full['wkv_t1'], full['wo1'], "l1_xa")
    x6, sv_ffn1 = _ffn_fwd(x5, inp['norm_ffn_g'][1], full['up_t1'], full['ffn_conv'][1], full['down1'], "l1_ffn")
    loss_part, dx6, dg_final = _loss_head(x6, inp['norm_final_g'], inp['loss_target'][0], "loss_head")

    dx5, g_ffn1 = _ffn_bwd(dx6, x5, inp['norm_ffn_g'][1], full['up_t1'], full['ffn_conv'][1], full['down1'], sv_ffn1, "l1_ffn")
    dx4, dmem1, g_xa1 = _xa_bwd(dx5, x4, inp['norm_xa_g'][1], mem_n, full['wq1'], full['wkv_t1'], full['wo1'], sv_xa1, "l1_xa")
    dxn1, dpool_w, dpool_scale = _pool_bwd(xn1, full['pool_w'], full['pool_scale'], dx4, "l1_pool_bwd")
    dx3, dg_mix1 = _rms_bwd(x3, inp['norm_mix_g'][1], dxn1, dx4, "l1_mix_norm_bwd")
    dx2, g_ffn0 = _ffn_bwd(dx3, x2, inp['norm_ffn_g'][0], full['up_t0'], full['ffn_conv'][0], full['down0'], sv_ffn0, "l0_ffn")
    dx1, dmem0, g_xa0 = _xa_bwd(dx2, x1, inp['norm_xa_g'][0], mem_n, full['wq0'], full['wkv_t0'], full['wo0'], sv_xa0, "l0_xa")
    dxn0, g_mix0 = _hybrid_bwd(dx1, xn0, wts0, p0, sv_mix0)
    grad_x, dg_mix0 = _rms_bwd(x0, inp['norm_mix_g'][0], dxn0, dx1, "l0_mix_norm_bwd")
    _, dg_mem = _rms_bwd(inp['mem'][0], inp['norm_mem_g'], dmem0 + dmem1, None, "mem_norm_bwd")

    db_in, dc_out, da_row, dd = g_mix0['s5']
    dlb_re, dlb_im, dbb_re, dbb_im, dc_re, dc_im = _s5_matrix_grads(db_in, dc_out, da_row)
    dlam_re, dlam_im, dbr, dbi, dlog_dt = disc_vjp((dlb_re, dlb_im, dbb_re, dbb_im))

    pieces = lambda a: a.reshape(N_DEV, a.shape[0] // N_DEV, a.shape[1])
    w_in_pieces = pieces(g_mix0['w_in_t'])
    w_in_pieces = jnp.concatenate(
        [w_in_pieces, jnp.zeros((N_DEV, PIECES[0][1] - W_IN_PIECE, D_MODEL), BF16)], axis=1)
    pool_pieces = jnp.swapaxes(dpool_w.astype(BF16).reshape(len(POOL_WINDOWS), N_DEV, -1, POOL_GROUP), 0, 1)
    big_grads = {
        'w_in_t': w_in_pieces, 'w_glu': g_mix0['w_glu_b'].astype(BF16).reshape(N_DEV, -1, PACK_COLS),
        'w_out': pieces(g_mix0['w_out_ab']), 'pool_w': pool_pieces.reshape(N_DEV, -1, PACK_COLS)}
    for l, (gx, gf) in enumerate(((g_xa0, g_ffn0), (g_xa1, g_ffn1))):
        big_grads['wq%d' % l], big_grads['wkv_t%d' % l] = pieces(gx['wq']), pieces(gx['wkv_t'])
        big_grads['wo%d' % l] = pieces(gx['wo'])
        big_grads['up_t%d' % l], big_grads['down%d' % l] = pieces(gf['w_up_t']), pieces(gf['w_down'])
    rep_grads = {
        'norm_mix_g': jnp.concatenate([dg_mix0, dg_mix1], 0), 'norm_xa_g': jnp.concatenate([g_xa0['g'], g_xa1['g']], 0),
        'norm_ffn_g': jnp.concatenate([g_ffn0['g'], g_ffn1['g']], 0), 'norm_mem_g': dg_mem.reshape(-1),
        'norm_final_g': dg_final.reshape(-1), 'a_log_a': g_mix0['a_log_a'], 'dt_bias_a': g_mix0['dt_bias_a'],
        'onorm_g_a': g_mix0['onorm_g_a'], 'ssm_lambda_re': dlam_re[None], 'ssm_lambda_im': dlam_im[None],
        'ssm_b_re': dbr[None], 'ssm_b_im': dbi[None], 'ssm_c_re': dc_re[None], 'ssm_c_im': dc_im[None],
        'ssm_d': dd.reshape(1, N_GROUPS, SSM_GROUP), 'ssm_log_dt': dlog_dt[None], 'b_glu_b': g_mix0['b_glu_b']}
    small_grads = {'conv_qkv_a': g_mix0['conv_qkv_a'][None], 'pool_scale': dpool_scale,
                   'ffn_conv': jnp.stack([g_ffn0['conv'], g_ffn1['conv']])}
    return loss_part, grad_x, big_grads, rep_grads, small_grads


def _reduce_gradients(big_grads, rep_grads, small_grads):
    for key, rows in PIECES:
        assert big_grads[key].shape == (N_DEV, rows, PACK_COLS) and big_grads[key].dtype == BF16, key
    piece_list = [big_grads[k] for k, _ in PIECES]
    core = lax.axis_index("c")
    keep = jnp.concatenate([lax.dynamic_index_in_dim(a.reshape(4, 2, a.shape[1], PACK_COLS), core, 1, keepdims=False)
                            for a in piece_list], axis=1)
    from_sibling = _swap_sibling(piece_list, "grads_to_sibling")
    chip_sums = _pair_sum(keep, from_sibling, "grads_pair_sum")
    from_chips = _exchange_chips(chip_sums, "grads_to_chips")
    big_reduced = _chip_sum(chip_sums, from_chips, "grads_chip_sum")

    misc_list = [rep_grads[n].astype(F32).reshape(-1) for n in REP_NAMES] + \
                [small_grads[n].astype(F32).reshape(-1) for n in SMALL_NAMES]
    misc_sizes = [int(a.shape[0]) for a in misc_list]
    misc_local = _pack_rows(misc_list, 8)
    (misc_all,) = _all_gather([misc_local], "gather_small_grads")
    misc_sum = _sum_leading(misc_all, "small_grads_sum")
    return big_reduced, misc_sum, misc_sizes


def _update(inp, loss_part, grad_x, big_reduced, misc_sum, misc_sizes):
    big_names, small_names, rep_names = BIG_NAMES, SMALL_NAMES, REP_NAMES
    misc_offs = _row_offsets(misc_sizes)
    dev = 4 * lax.axis_index("x") + 2 * lax.axis_index("y") + lax.axis_index("c")
    piece_rows = dict(PIECES)
    piece = lambda key, rows=None: big_reduced[PIECE_OFFS[key]:PIECE_OFFS[key] + (rows or piece_rows[key])]
    both = lambda name, fn: jnp.stack([fn(piece(name + '0')), fn(piece(name + '1'))])
    ident, transpose = (lambda a: a), (lambda a: a.T)
    grads = {'w_in_ab': piece('w_in_t', W_IN_PIECE).T[None], 'w_glu_b': piece('w_glu').reshape(inp['w_glu_b'].shape),
             'w_out_ab': piece('w_out')[None], 'pool_w': piece('pool_w').reshape(inp['pool_w'].shape),
             'xa_wq': both('wq', ident), 'xa_wkv': both('wkv_t', transpose), 'xa_wo': both('wo', ident),
             'ffn_w_up': both('up_t', transpose), 'ffn_w_down': both('down', ident)}
    for n, off, size in zip(rep_names + small_names, misc_offs, misc_sizes):
        flat = misc_sum[off:off + _round_up(size, PACK_COLS) // PACK_COLS].reshape(-1)[:size]
        if n in REPLICATED:
            grads[n] = flat.reshape(inp[n].shape)
        else:
            shape, axis = SMALL_SHARDED[n]
            grads[n] = lax.dynamic_index_in_dim(_split_shards(flat.reshape(shape), axis), dev, 0, keepdims=False
                                                ).reshape(inp[n].shape)
    tiny_names = [n for n in WEIGHT_NAMES if n not in BIG_SHARDED]
    upd = {}
    for n in big_names:
        upd[n] = _adamw(inp[n], grads[n], inp['m_' + n], inp['v_' + n], "adamw_" + n)
    tiny_sizes = [int(np.prod(inp[n].shape)) for n in tiny_names]
    tiny_offs = _row_offsets(tiny_sizes)
    packs = [_pack_rows([src(n).astype(F32).reshape(-1) for n in tiny_names], 8)
             for src in (lambda n: inp[n], lambda n: grads[n], lambda n: inp['m_' + n], lambda n: inp['v_' + n])]
    tiny_out = _adamw(*packs, "adamw_small")
    for n, off, size in zip(tiny_names, tiny_offs, tiny_sizes):
        r = _round_up(size, PACK_COLS) // PACK_COLS
        upd[n] = tuple(o[off:off + r].reshape(-1)[:size].reshape(inp[n].shape) for o in tiny_out)

    loss = lax.psum(loss_part[0, 0], ("x", "y", "c"))
    outs = [loss, grad_x[None]]
    outs += [grads[n] for n in WEIGHT_NAMES]
    for i in range(3):
        outs += [upd[n][i] for n in WEIGHT_NAMES]
    return tuple(outs)


def _step(inp):
    loss_part, grad_x, big_grads, rep_grads, small_grads = _local_step(inp)
    big_reduced, misc_sum, misc_sizes = _reduce_gradients(big_grads, rep_grads, small_grads)
    return _update(inp, loss_part, grad_x, big_reduced, misc_sum, misc_sizes)


INPUT_NAMES = (['x', 'mem'] + WEIGHT_NAMES + ['loss_target'] + ['m_' + n for n in WEIGHT_NAMES]
               + ['v_' + n for n in WEIGHT_NAMES])


def kernel(x, mem, norm_mix_g, norm_xa_g, norm_ffn_g, norm_mem_g, norm_final_g, w_in_ab, conv_qkv_a, a_log_a, dt_bias_a, onorm_g_a, ssm_lambda_re, ssm_lambda_im, ssm_b_re, ssm_b_im, ssm_c_re, ssm_c_im, ssm_d, ssm_log_dt, w_glu_b, b_glu_b, w_out_ab, pool_w, pool_scale, xa_wq, xa_wkv, xa_wo, ffn_w_up, ffn_conv, ffn_w_down, loss_target, m_norm_mix_g, m_norm_xa_g, m_norm_ffn_g, m_norm_mem_g, m_norm_final_g, m_w_in_ab, m_conv_qkv_a, m_a_log_a, m_dt_bias_a, m_onorm_g_a, m_ssm_lambda_re, m_ssm_lambda_im, m_ssm_b_re, m_ssm_b_im, m_ssm_c_re, m_ssm_c_im, m_ssm_d, m_ssm_log_dt, m_w_glu_b, m_b_glu_b, m_w_out_ab, m_pool_w, m_pool_scale, m_xa_wq, m_xa_wkv, m_xa_wo, m_ffn_w_up, m_ffn_conv, m_ffn_w_down, v_norm_mix_g, v_norm_xa_g, v_norm_ffn_g, v_norm_mem_g, v_norm_final_g, v_w_in_ab, v_conv_qkv_a, v_a_log_a, v_dt_bias_a, v_onorm_g_a, v_ssm_lambda_re, v_ssm_lambda_im, v_ssm_b_re, v_ssm_b_im, v_ssm_c_re, v_ssm_c_im, v_ssm_d, v_ssm_log_dt, v_w_glu_b, v_b_glu_b, v_w_out_ab, v_pool_w, v_pool_scale, v_xa_wq, v_xa_wkv, v_xa_wo, v_ffn_w_up, v_ffn_conv, v_ffn_w_down):
    args = (x, mem, norm_mix_g, norm_xa_g, norm_ffn_g, norm_mem_g, norm_final_g, w_in_ab, conv_qkv_a, a_log_a, dt_bias_a, onorm_g_a, ssm_lambda_re, ssm_lambda_im, ssm_b_re, ssm_b_im, ssm_c_re, ssm_c_im, ssm_d, ssm_log_dt, w_glu_b, b_glu_b, w_out_ab, pool_w, pool_scale, xa_wq, xa_wkv, xa_wo, ffn_w_up, ffn_conv, ffn_w_down, loss_target, m_norm_mix_g, m_norm_xa_g, m_norm_ffn_g, m_norm_mem_g, m_norm_final_g, m_w_in_ab, m_conv_qkv_a, m_a_log_a, m_dt_bias_a, m_onorm_g_a, m_ssm_lambda_re, m_ssm_lambda_im, m_ssm_b_re, m_ssm_b_im, m_ssm_c_re, m_ssm_c_im, m_ssm_d, m_ssm_log_dt, m_w_glu_b, m_b_glu_b, m_w_out_ab, m_pool_w, m_pool_scale, m_xa_wq, m_xa_wkv, m_xa_wo, m_ffn_w_up, m_ffn_conv, m_ffn_w_down, v_norm_mix_g, v_norm_xa_g, v_norm_ffn_g, v_norm_mem_g, v_norm_final_g, v_w_in_ab, v_conv_qkv_a, v_a_log_a, v_dt_bias_a, v_onorm_g_a, v_ssm_lambda_re, v_ssm_lambda_im, v_ssm_b_re, v_ssm_b_im, v_ssm_c_re, v_ssm_c_im, v_ssm_d, v_ssm_log_dt, v_w_glu_b, v_b_glu_b, v_w_out_ab, v_pool_w, v_pool_scale, v_xa_wq, v_xa_wkv, v_xa_wo, v_ffn_w_up, v_ffn_conv, v_ffn_w_down)
    return _step(dict(zip(INPUT_NAMES, args)))
```

```python
import functools
import math

import numpy as np
import jax
import jax.numpy as jnp
from jax import lax
from jax.experimental import pallas as pl
from jax.experimental.pallas import tpu as pltpu

F32, BF16 = jnp.float32, jnp.bfloat16
HIGH, HIGHEST = lax.Precision.HIGH, lax.Precision.HIGHEST
MESH = pl.DeviceIdType.MESH

N_DEV = 8
SEQ, D_MODEL, MEM_LEN = 2048, 1024, 256
WIDTH_A, N_HEADS_A, HEAD_A, CONV_A = 512, 4, 128, 4
GDR_CHUNK = 128
SSM_WIDTH, SSM_GROUP, N_GROUPS, SSM_STATE = 512, 16, 32, 64
SSM_CH = N_GROUPS * SSM_STATE
SCAN_CB = 512
POOL_WINDOWS = (2, 4, 8, 16)
POOL_GROUP = 256
N_HEADS_X, HEAD_X = 4, 256
D_FF, CONV_FFN = 2816, 3
RMS_EPS = 1e-6
ADAM_LR, ADAM_B1, ADAM_B2, ADAM_EPS, ADAM_WD, ADAM_STEP = 0.001, 0.9, 0.999, 1e-08, 0.01, 10
LANE = 128
PACK_COLS = 1024
VMEM_LIMIT_BYTES = 56 * 1024 * 1024


def _params(sem=None):
    return pltpu.CompilerParams(dimension_semantics=sem, vmem_limit_bytes=VMEM_LIMIT_BYTES)


class Comm:
    def __init__(self, inputs, out_shapes, sems, start, end, mid=None):
        self.inputs, self.out_shapes, self.sems = list(inputs), list(out_shapes), list(sems)
        self.start, self.mid, self.end = start, mid, end


def _merge_comms(comms):
    comms = [c for c in comms if c is not None]
    if not comms:
        return None, []
    bounds, ni, no, ns = [], 0, 0, 0
    for c in comms:
        bounds.append((ni, no, ns))
        ni, no, ns = ni + len(c.inputs), no + len(c.out_shapes), ns + len(c.sems)

    def phase(which):
        def run(ins, outs, sems):
            for c, (i0, o0, s0) in zip(comms, bounds):
                fn = getattr(c, which)
                if fn is not None:
                    fn(ins[i0:i0 + len(c.inputs)], outs[o0:o0 + len(c.out_shapes)], sems[s0:s0 + len(c.sems)])
        return run

    merged = Comm([a for c in comms for a in c.inputs], [s for c in comms for s in c.out_shapes],
                  [s for c in comms for s in c.sems], phase("start"), phase("end"), phase("mid"))
    return merged, [(o0, o0 + len(c.out_shapes)) for c, (_, o0, _) in zip(comms, bounds)]


def _call(body, *, name, grid, in_specs, out_specs, out_shape, args, scratch_shapes=(), sem=None, comm=None):
    single = not isinstance(out_shape, (list, tuple))
    out_specs_l = [out_specs] if single else list(out_specs)
    out_shape_l = [out_shape] if single else list(out_shape)
    scratch_shapes = list(scratch_shapes)
    merged, spans = _merge_comms(comm if isinstance(comm, (list, tuple)) else [comm])
    if merged is None:
        outs = pl.pallas_call(body, name=name, grid=grid, in_specs=list(in_specs), out_specs=out_specs_l,
                              out_shape=out_shape_l, scratch_shapes=scratch_shapes, compiler_params=_params(sem))(*args)
        outs = outs[0] if single else outs
        return outs if comm is None else (outs, [])
    n_in, n_out, n_scr = len(in_specs), len(out_specs_l), len(scratch_shapes)
    ci, co = len(merged.inputs), len(merged.out_shapes)
    total = int(np.prod(grid))

    def wrapped(*refs):
        ins, cins = refs[:n_in], refs[n_in:n_in + ci]
        outs, couts = refs[n_in + ci:n_in + ci + n_out], refs[n_in + ci + n_out:n_in + ci + n_out + co]
        scr, csems = refs[n_in + ci + n_out + co:n_in + ci + n_out + co + n_scr], refs[n_in + ci + n_out + co + n_scr:]
        lin = pl.program_id(0)
        for d in range(1, len(grid)):
            lin = lin * grid[d] + pl.program_id(d)
        pl.when(lin == 0)(lambda: merged.start(cins, couts, csems))
        body(*ins, *outs, *scr)
        def finish():
            merged.mid(cins, couts, csems)
            merged.end(cins, couts, csems)

        pl.when(lin == total - 1)(finish)

    any_spec = pl.BlockSpec(memory_space=pl.ANY)
    res = pl.pallas_call(
        wrapped, name=name, grid=grid, in_specs=list(in_specs) + [any_spec] * ci,
        out_specs=out_specs_l + [any_spec] * co, out_shape=out_shape_l + merged.out_shapes,
        scratch_shapes=scratch_shapes + merged.sems,
        compiler_params=_params(("arbitrary",) * len(grid)))(*args, *merged.inputs)
    outs, couts = res[:n_out], res[n_out:]
    return (outs[0] if single else list(outs)), [list(couts[a:b]) for a, b in spans]


def _comm_only(comm, name):
    def body():
        pass

    _, couts = _call(body, name=name, grid=(1,), in_specs=[], out_specs=[], out_shape=[], args=[], comm=comm)
    return couts[0]


def _tile(dim, pref):
    best = None
    for t in range(LANE, min(dim, pref) + 1, LANE):
        if dim % t == 0:
            best = t
    return best if best is not None else dim


MM_VMEM_BUDGET = 40 * 1024 * 1024


def _mm_tiles(m, n, k, a_bytes, b_bytes, o_bytes, r_bytes):
    for tk in (k, _tile(k, 2048), _tile(k, 1024), _tile(k, 512)):
        for tm, tn in ((1024, 1536), (1024, 1024), (1024, 512), (512, 512), (256, 512), (256, 256)):
            tm, tn = _tile(m, tm), _tile(n, tn)
            acc = 0 if tk == k else tm * tn * 4
            need = 2 * (tm * tk * a_bytes + tk * tn * b_bytes + tm * tn * (o_bytes + r_bytes)) + acc
            if need <= MM_VMEM_BUDGET:
                return tm, tn, tk
    raise ValueError("no matmul tiling fits VMEM")


def _mm(a, b, mode, name, out_dtype=F32, res=None, comm=None):
    if mode == "nn":
        (m, k), n = a.shape, b.shape[1]
    elif mode == "nt":
        (m, k), n = a.shape, b.shape[0]
    else:
        (k, m), n = a.shape, b.shape[1]
    tm, tn, tk = _mm_tiles(m, n, k, a.dtype.itemsize, b.dtype.itemsize, jnp.dtype(out_dtype).itemsize,
                           0 if res is None else res.dtype.itemsize)
    nk = k // tk
    dims = {"nn": ((1,), (0,)), "nt": ((1,), (1,)), "tn": ((0,), (0,))}[mode]

    def body(*refs):
        if res is None:
            a_ref, b_ref, o_ref = refs[:3]
            r_ref = None
        else:
            a_ref, b_ref, r_ref, o_ref = refs[:4]
        part = lax.dot_general(a_ref[...].astype(BF16), b_ref[...].astype(BF16), (dims, ((), ())),
                               preferred_element_type=F32)

        def finish(out):
            if r_ref is not None:
                out = out + r_ref[...].astype(F32)
            o_ref[...] = out.astype(out_dtype)

        if nk == 1:
            finish(part)
            return
        acc = refs[-1]
        kk = pl.program_id(2)

        @pl.when(kk == 0)
        def _():
            acc[...] = part

        @pl.when(kk > 0)
        def _():
            acc[...] += part

        @pl.when(kk == nk - 1)
        def _():
            finish(acc[...])

    a_spec = (pl.BlockSpec((tk, tm), lambda i, j, q: (q, i)) if mode == "tn"
              else pl.BlockSpec((tm, tk), lambda i, j, q: (i, q)))
    b_spec = (pl.BlockSpec((tn, tk), lambda i, j, q: (j, q)) if mode == "nt"
              else pl.BlockSpec((tk, tn), lambda i, j, q: (q, j)))
    o_spec = pl.BlockSpec((tm, tn), lambda i, j, q: (i, j))
    in_specs, args = [a_spec, b_spec], [a, b]
    if res is not None:
        in_specs.append(o_spec)
        args.append(res)
    return _call(body, name=name, grid=(m // tm, n // tn, nk), in_specs=in_specs, out_specs=o_spec,
                 out_shape=jax.ShapeDtypeStruct((m, n), out_dtype),
                 scratch_shapes=[] if nk == 1 else [pltpu.VMEM((tm, tn), F32)],
                 sem=("parallel", "parallel", "arbitrary"), args=args, comm=comm)


def _rms_fwd(x, g, out_dtype, name, tr=256):
    rows, d = x.shape

    def body(x_ref, g_ref, o_ref):
        xv = x_ref[...]
        r = lax.rsqrt(jnp.mean(xv * xv, axis=-1, keepdims=True) + RMS_EPS)
        o_ref[...] = (xv * r * g_ref[...]).astype(out_dtype)

    return pl.pallas_call(
        body, name=name, grid=(rows // tr,),
        in_specs=[pl.BlockSpec((tr, d), lambda i: (i, 0)), pl.BlockSpec((1, d), lambda i: (0, 0))],
        out_specs=pl.BlockSpec((tr, d), lambda i: (i, 0)), out_shape=jax.ShapeDtypeStruct((rows, d), out_dtype),
        compiler_params=_params(("parallel",)))(x, g.reshape(1, d))


def _rms_bwd(x, g, dy, dres, name, tr=256, comm=None):
    rows, d = x.shape

    def body(*refs):
        if dres is None:
            x_ref, g_ref, dy_ref, dx_ref, dg_ref = refs
            r_ref = None
        else:
            x_ref, g_ref, dy_ref, r_ref, dx_ref, dg_ref = refs

        @pl.when(pl.program_id(0) == 0)
        def _():
            dg_ref[...] = jnp.zeros_like(dg_ref)

        xv, dyv = x_ref[...], dy_ref[...].astype(F32)
        r = lax.rsqrt(jnp.mean(xv * xv, axis=-1, keepdims=True) + RMS_EPS)
        xh = xv * r
        dyg = dyv * g_ref[...]
        dx = r * (dyg - xh * jnp.mean(dyg * xh, axis=-1, keepdims=True))
        if r_ref is not None:
            dx = dx + r_ref[...]
        dx_ref[...] = dx
        dg_ref[...] += jnp.sum(dyv * xh, axis=0, keepdims=True)

    blk = pl.BlockSpec((tr, d), lambda i: (i, 0))
    vec = pl.BlockSpec((1, d), lambda i: (0, 0))
    in_specs, args = [blk, vec, blk], [x, g.reshape(1, d), dy]
    if dres is not None:
        in_specs.append(blk)
        args.append(dres)
    return _call(
        body, name=name, grid=(rows // tr,), in_specs=in_specs, out_specs=[blk, vec],
        out_shape=[jax.ShapeDtypeStruct((rows, d), F32), jax.ShapeDtypeStruct((1, d), F32)],
        sem=("arbitrary",), args=args, comm=comm)


def _loss_head(x, g, target, name, tr=256):
    rows, d = x.shape

    def body(x_ref, g_ref, t_ref, loss_ref, dx_ref, dg_ref):
        @pl.when(pl.program_id(0) == 0)
        def _():
            dg_ref[...] = jnp.zeros_like(dg_ref)
            loss_ref[...] = jnp.zeros_like(loss_ref)

        xv = x_ref[...]
        r = lax.rsqrt(jnp.mean(xv * xv, axis=-1, keepdims=True) + RMS_EPS)
        xh = xv * r
        err = xh * g_ref[...] - t_ref[...]
        loss_ref[...] += 0.5 * jnp.sum(jnp.mean(err * err, axis=-1, keepdims=True), keepdims=True)
        dyv = err * (1.0 / d)
        dyg = dyv * g_ref[...]
        dx_ref[...] = r * (dyg - xh * jnp.mean(dyg * xh, axis=-1, keepdims=True))
        dg_ref[...] += jnp.sum(dyv * xh, axis=0, keepdims=True)

    blk = pl.BlockSpec((tr, d), lambda i: (i, 0))
    vec = pl.BlockSpec((1, d), lambda i: (0, 0))
    return pl.pallas_call(
        body, name=name, grid=(rows // tr,), in_specs=[blk, vec, blk],
        out_specs=[pl.BlockSpec((1, 1), lambda i: (0, 0)), blk, vec],
        out_shape=[jax.ShapeDtypeStruct((1, 1), F32), jax.ShapeDtypeStruct((rows, d), F32),
                   jax.ShapeDtypeStruct((1, d), F32)],
        compiler_params=_params(("arbitrary",)))(x, g.reshape(1, d), target)


def _shift_down(x, s):
    rows = lax.broadcasted_iota(jnp.int32, x.shape, 0)
    return jnp.where(rows >= s, pltpu.roll(x, s, 0), 0.0)


def _shift_up(x, s):
    n = x.shape[0]
    rows = lax.broadcasted_iota(jnp.int32, x.shape, 0)
    return jnp.where(rows < n - s, pltpu.roll(x, n - s, 0), 0.0)


def _sigmoid(x):
    return 1.0 / (1.0 + jnp.exp(-x))


def _silu_and_grad(x):
    s = _sigmoid(x)
    return x * s, s * (1.0 + x * (1.0 - s))


_GELU_C0, _GELU_C1 = math.sqrt(2.0 / math.pi), 0.044715


def _gelu_and_grad(x):
    th = jnp.tanh(_GELU_C0 * (x + _GELU_C1 * x * x * x))
    y = 0.5 * x * (1.0 + th)
    dy = 0.5 * (1.0 + th) + 0.5 * x * (1.0 - th * th) * _GELU_C0 * (1.0 + 3.0 * _GELU_C1 * x * x)
    return y, dy


def _ffn_act_fwd(h, w, name, tc=256, comm=None):
    t = h.shape[0]
    nb = D_FF // tc

    def body(hg_ref, hv_ref, wg_ref, wv_ref, a_ref):
        def conv(x, wr):
            return wr[2:3, :] * x + wr[1:2, :] * _shift_down(x, 1) + wr[0:1, :] * _shift_down(x, 2)

        cg = conv(hg_ref[...], wg_ref[...])
        cv = conv(hv_ref[...], wv_ref[...])
        a_ref[...] = (cg * _sigmoid(cg) * cv).astype(BF16)

    return _call(
        body, name=name, grid=(nb,),
        in_specs=[pl.BlockSpec((t, tc), lambda j: (0, j)), pl.BlockSpec((t, tc), lambda j: (0, j + nb)),
                  pl.BlockSpec((CONV_FFN, tc), lambda j: (0, j)), pl.BlockSpec((CONV_FFN, tc), lambda j: (0, j + nb))],
        out_specs=pl.BlockSpec((t, tc), lambda j: (0, j)), out_shape=jax.ShapeDtypeStruct((t, D_FF), BF16),
        sem=("parallel",), args=(h, h, w, w), comm=comm)


def _ffn_act_bwd(h, w, da, name, tc=256, comm=None):
    t = h.shape[0]
    nb = D_FF // tc

    def body(hg_ref, hv_ref, wg_ref, wv_ref, da_ref, dhg_ref, dhv_ref, dwg_ref, dwv_ref):
        hg, hv, wg, wv = hg_ref[...], hv_ref[...], wg_ref[...], wv_ref[...]
        hg1, hg2, hv1, hv2 = _shift_down(hg, 1), _shift_down(hg, 2), _shift_down(hv, 1), _shift_down(hv, 2)
        cg = wg[2:3, :] * hg + wg[1:2, :] * hg1 + wg[0:1, :] * hg2
        cv = wv[2:3, :] * hv + wv[1:2, :] * hv1 + wv[0:1, :] * hv2
        sg, dsg = _silu_and_grad(cg)
        dav = da_ref[...].astype(F32)
        dcv = dav * sg
        dcg = dav * cv * dsg

        def conv_t(dc, wr):
            return wr[2:3, :] * dc + wr[1:2, :] * _shift_up(dc, 1) + wr[0:1, :] * _shift_up(dc, 2)

        dhg_ref[...] = conv_t(dcg, wg).astype(BF16)
        dhv_ref[...] = conv_t(dcv, wv).astype(BF16)
        dwg_ref[0:1, :] = jnp.sum(dcg * hg2, axis=0, keepdims=True)
        dwg_ref[1:2, :] = jnp.sum(dcg * hg1, axis=0, keepdims=True)
        dwg_ref[2:3, :] = jnp.sum(dcg * hg, axis=0, keepdims=True)
        dwv_ref[0:1, :] = jnp.sum(dcv * hv2, axis=0, keepdims=True)
        dwv_ref[1:2, :] = jnp.sum(dcv * hv1, axis=0, keepdims=True)
        dwv_ref[2:3, :] = jnp.sum(dcv * hv, axis=0, keepdims=True)

    big = lambda off: pl.BlockSpec((t, tc), lambda j: (0, j + off))
    small = lambda off: pl.BlockSpec((CONV_FFN, tc), lambda j: (0, j + off))
    res = _call(
        body, name=name, grid=(nb,),
        in_specs=[big(0), big(nb), small(0), small(nb), big(0)],
        out_specs=[big(0), big(0), small(0), small(0)],
        out_shape=[jax.ShapeDtypeStruct((t, D_FF), BF16), jax.ShapeDtypeStruct((t, D_FF), BF16),
                   jax.ShapeDtypeStruct((CONV_FFN, D_FF), F32), jax.ShapeDtypeStruct((CONV_FFN, D_FF), F32)],
        sem=("parallel",), args=(h, h, w, w, da), comm=comm)
    (dhg, dhv, dwg, dwv), couts = res if comm is not None else (res, None)
    out = (jnp.concatenate([dhg, dhv], axis=1), jnp.concatenate([dwg, dwv], axis=1))
    return out if comm is None else (out, couts)


def _attn_probs(q, k):
    s = lax.dot_general(q.astype(BF16), k.astype(BF16), (((1,), (1,)), ((), ())),
                        preferred_element_type=F32) * (HEAD_X ** -0.5)
    s = s - jnp.max(s, axis=-1, keepdims=True)
    p = jnp.exp(s)
    return p / jnp.sum(p, axis=-1, keepdims=True)


def _attn_fwd(q, kv, name, tq=512, comm=None):
    t = q.shape[0]

    def body(q_ref, k_ref, v_ref, o_ref):
        p = _attn_probs(q_ref[...], k_ref[...])
        o_ref[...] = jnp.dot(p.astype(BF16), v_ref[...].astype(BF16), preferred_element_type=F32).astype(BF16)

    return _call(
        body, name=name, grid=(N_HEADS_X, t // tq),
        in_specs=[pl.BlockSpec((tq, HEAD_X), lambda h, i: (i, h)),
                  pl.BlockSpec((MEM_LEN, HEAD_X), lambda h, i: (0, h)),
                  pl.BlockSpec((MEM_LEN, HEAD_X), lambda h, i: (0, h + N_HEADS_X))],
        out_specs=pl.BlockSpec((tq, HEAD_X), lambda h, i: (i, h)),
        out_shape=jax.ShapeDtypeStruct((t, N_HEADS_X * HEAD_X), BF16),
        sem=("parallel", "parallel"), args=(q, kv, kv), comm=comm)


def _attn_bwd(q, kv, do, name, tq=512):
    t = q.shape[0]

    def body(q_ref, k_ref, v_ref, do_ref, dq_ref, dk_ref, dv_ref):
        @pl.when(pl.program_id(1) == 0)
        def _():
            dk_ref[...] = jnp.zeros_like(dk_ref)
            dv_ref[...] = jnp.zeros_like(dv_ref)

        qb, kb, vb, dob = (r[...].astype(BF16) for r in (q_ref, k_ref, v_ref, do_ref))
        p = _attn_probs(qb, kb)
        dp = lax.dot_general(dob, vb, (((1,), (1,)), ((), ())), preferred_element_type=F32)
        ds = p * (dp - jnp.sum(dp * p, axis=-1, keepdims=True)) * (HEAD_X ** -0.5)
        dsb = ds.astype(BF16)
        dq_ref[...] = jnp.dot(dsb, kb, preferred_element_type=F32).astype(BF16)
        dk_ref[...] += lax.dot_general(dsb, qb, (((0,), (0,)), ((), ())), preferred_element_type=F32)
        dv_ref[...] += lax.dot_general(p.astype(BF16), dob, (((0,), (0,)), ((), ())), preferred_element_type=F32)

    qs = pl.BlockSpec((tq, HEAD_X), lambda h, i: (i, h))
    ms = pl.BlockSpec((MEM_LEN, HEAD_X), lambda h, i: (0, h))
    return pl.pallas_call(
        body, name=name, grid=(N_HEADS_X, t // tq),
        in_specs=[qs, ms, pl.BlockSpec((MEM_LEN, HEAD_X), lambda h, i: (0, h + N_HEADS_X)), qs],
        out_specs=[qs, ms, ms],
        out_shape=[jax.ShapeDtypeStruct((t, D_MODEL), BF16), jax.ShapeDtypeStruct((MEM_LEN, D_MODEL), F32),
                   jax.ShapeDtypeStruct((MEM_LEN, D_MODEL), F32)],
        compiler_params=_params(("parallel", "arbitrary")))(q, kv, kv, do)


def _pool_counts(t, win):
    pos = lax.broadcasted_iota(jnp.int32, (t, 1), 0).astype(F32) + 1.0
    return 1.0 / jnp.minimum(pos, float(win))


def _pool_delta(xv, win):
    s, step = xv, 1
    while step < win:
        s = s + _shift_down(s, step)
        step *= 2
    return s * _pool_counts(xv.shape[0], win) - xv


def _pool_delta_t(dv, win):
    s, step = dv * _pool_counts(dv.shape[0], win), 1
    while step < win:
        s = s + _shift_up(s, step)
        step *= 2
    return s - dv


def _pool_fwd(xn, w, scale, res, name):
    t = xn.shape[0]

    def make_branch(win, xn_ref, w_ref, s_ref, r_ref, o_ref):
        def branch():
            dl = _pool_delta(xn_ref[...], win)
            y = jnp.dot(dl.astype(BF16), w_ref[0], preferred_element_type=F32)
            o_ref[...] = r_ref[...] + y * s_ref[...]
        return branch

    def body(xn_ref, w_ref, s_ref, r_ref, o_ref):
        for gi, win in enumerate(POOL_WINDOWS):
            pl.when(pl.program_id(0) == gi)(make_branch(win, xn_ref, w_ref, s_ref, r_ref, o_ref))

    blk = pl.BlockSpec((t, POOL_GROUP), lambda g: (0, g))
    return pl.pallas_call(
        body, name=name, grid=(len(POOL_WINDOWS),),
        in_specs=[blk, pl.BlockSpec((1, POOL_GROUP, POOL_GROUP), lambda g: (g, 0, 0)),
                  pl.BlockSpec((1, POOL_GROUP), lambda g: (0, g)), blk],
        out_specs=blk, out_shape=jax.ShapeDtypeStruct((t, D_MODEL), F32),
        compiler_params=_params(("parallel",)))(xn, w, scale, res)


def _pool_bwd(xn, w, scale, dmix, name):
    t = xn.shape[0]

    def make_branch(win, xn_ref, w_ref, s_ref, d_ref, dxn_ref, dw_ref, ds_ref):
        def branch():
            dl = _pool_delta(xn_ref[...], win).astype(BF16)
            wv = w_ref[0]
            dm = d_ref[...]
            y = jnp.dot(dl, wv, preferred_element_type=F32)
            ds_ref[...] = jnp.sum(dm * y, axis=0, keepdims=True)
            dy = (dm * s_ref[...]).astype(BF16)
            dw_ref[0] = lax.dot_general(dl, dy, (((0,), (0,)), ((), ())), preferred_element_type=F32)
            ddl = lax.dot_general(dy, wv, (((1,), (1,)), ((), ())), preferred_element_type=F32)
            dxn_ref[...] = _pool_delta_t(ddl, win)
        return branch

    def body(*refs):
        for gi, win in enumerate(POOL_WINDOWS):
            pl.when(pl.program_id(0) == gi)(make_branch(win, *refs))

    blk = pl.BlockSpec((t, POOL_GROUP), lambda g: (0, g))
    wspec = pl.BlockSpec((1, POOL_GROUP, POOL_GROUP), lambda g: (g, 0, 0))
    vec = pl.BlockSpec((1, POOL_GROUP), lambda g: (0, g))
    return pl.pallas_call(
        body, name=name, grid=(len(POOL_WINDOWS),), in_specs=[blk, wspec, vec, blk], out_specs=[blk, wspec, vec],
        out_shape=[jax.ShapeDtypeStruct((t, D_MODEL), F32),
                   jax.ShapeDtypeStruct((len(POOL_WINDOWS), POOL_GROUP, POOL_GROUP), F32),
                   jax.ShapeDtypeStruct((1, D_MODEL), F32)],
        compiler_params=_params(("parallel",)))(xn, w, scale, dmix)


def _qkv_conv(h, wr):
    return (wr[3:4, :] * h + wr[2:3, :] * _shift_down(h, 1) + wr[1:2, :] * _shift_down(h, 2)
            + wr[0:1, :] * _shift_down(h, 3))


def _qkv_pre_fwd(h, w, col0, ncols, normalize, scale, name):
    t = h.shape[0]

    def body(h_ref, w_ref, o_ref):
        c = _qkv_conv(h_ref[...], w_ref[...])
        s = c * _sigmoid(c)
        if normalize:
            s = s * lax.rsqrt(jnp.sum(s * s, axis=-1, keepdims=True) + 1e-6) * scale
        o_ref[...] = s

    return pl.pallas_call(
        body, name=name, grid=(ncols,),
        in_specs=[pl.BlockSpec((t, HEAD_A), lambda j: (0, j + col0)), pl.BlockSpec((CONV_A, HEAD_A), lambda j: (0, j + col0))],
        out_specs=pl.BlockSpec((t, HEAD_A), lambda j: (0, j)), out_shape=jax.ShapeDtypeStruct((t, ncols * HEAD_A), F32),
        compiler_params=_params(("parallel",)))(h, w)


def _qkv_pre_bwd(h, w, dy, col0, ncols, normalize, scale, name):
    t = h.shape[0]

    def body(h_ref, w_ref, dy_ref, dh_ref, dw_ref):
        hv, wr, dyv = h_ref[...], w_ref[...], dy_ref[...]
        h1, h2, h3 = _shift_down(hv, 1), _shift_down(hv, 2), _shift_down(hv, 3)
        c = wr[3:4, :] * hv + wr[2:3, :] * h1 + wr[1:2, :] * h2 + wr[0:1, :] * h3
        s, dsilu = _silu_and_grad(c)
        if normalize:
            r = lax.rsqrt(jnp.sum(s * s, axis=-1, keepdims=True) + 1e-6)
            y = s * r
            dyv = dyv * scale
            ds = r * (dyv - y * jnp.sum(dyv * y, axis=-1, keepdims=True))
        else:
            ds = dyv
        dc = ds * dsilu
        dh = (wr[3:4, :] * dc + wr[2:3, :] * _shift_up(dc, 1) + wr[1:2, :] * _shift_up(dc, 2)
              + wr[0:1, :] * _shift_up(dc, 3))
        dh_ref[...] = dh.astype(BF16)
        dw_ref[0:1, :] = jnp.sum(dc * h3, axis=0, keepdims=True)
        dw_ref[1:2, :] = jnp.sum(dc * h2, axis=0, keepdims=True)
        dw_ref[2:3, :] = jnp.sum(dc * h1, axis=0, keepdims=True)
        dw_ref[3:4, :] = jnp.sum(dc * hv, axis=0, keepdims=True)

    return pl.pallas_call(
        body, name=name, grid=(ncols,),
        in_specs=[pl.BlockSpec((t, HEAD_A), lambda j: (0, j + col0)), pl.BlockSpec((CONV_A, HEAD_A), lambda j: (0, j + col0)),
                  pl.BlockSpec((t, HEAD_A), lambda j: (0, j))],
        out_specs=[pl.BlockSpec((t, HEAD_A), lambda j: (0, j)), pl.BlockSpec((CONV_A, HEAD_A), lambda j: (0, j))],
        out_shape=[jax.ShapeDtypeStruct((t, ncols * HEAD_A), BF16), jax.ShapeDtypeStruct((CONV_A, ncols * HEAD_A), F32)],
        compiler_params=_params(("parallel",)))(h, w, dy)


def _softplus(x):
    return jnp.maximum(x, 0.0) + jnp.log1p(jnp.exp(-jnp.abs(x)))


def _gates_fwd(ba, arow, brow, name):
    t = ba.shape[0]

    def body(x_ref, a_ref, b_ref, o_ref):
        xv = x_ref[...]
        lane = lax.broadcasted_iota(jnp.int32, xv.shape, 1)
        beta = _sigmoid(xv)
        g = -jnp.exp(a_ref[...]) * _softplus(xv + b_ref[...])
        o_ref[...] = jnp.where(lane < N_HEADS_A, beta, jnp.where(lane < 2 * N_HEADS_A, g, 0.0))

    return pl.pallas_call(body, name=name, out_shape=jax.ShapeDtypeStruct((t, LANE), F32),
                          compiler_params=_params())(ba, arow, brow)


def _gates_bwd(ba, arow, brow, dgb, name):
    t = ba.shape[0]

    def body(x_ref, a_ref, b_ref, d_ref, dx_ref, da_ref, db_ref):
        xv, dv = x_ref[...], d_ref[...]
        lane = lax.broadcasted_iota(jnp.int32, xv.shape, 1)
        beta = _sigmoid(xv)
        ea = jnp.exp(a_ref[...])
        z = xv + b_ref[...]
        dgv = jnp.where((lane >= N_HEADS_A) & (lane < 2 * N_HEADS_A), dv, 0.0) * (-ea)
        dz = dgv * _sigmoid(z)
        dx = jnp.where(lane < N_HEADS_A, dv * beta * (1.0 - beta), dz)
        dx_ref[...] = dx.astype(BF16)
        db_ref[...] = jnp.sum(dz, axis=0, keepdims=True)
        da_ref[...] = jnp.sum(dgv * _softplus(z), axis=0, keepdims=True)

    return pl.pallas_call(
        body, name=name,
        out_shape=[jax.ShapeDtypeStruct((t, LANE), BF16), jax.ShapeDtypeStruct((1, LANE), F32),
                   jax.ShapeDtypeStruct((1, LANE), F32)],
        compiler_params=_params())(ba, arow, brow, dgb)


def _dot(a, b, prec=None):
    if prec is None:
        return jnp.dot(a.astype(BF16), b.astype(BF16), preferred_element_type=F32)
    return jnp.dot(a, b, precision=prec, preferred_element_type=F32)


def _dot_nt(a, b, prec=None):
    if prec is None:
        a, b = a.astype(BF16), b.astype(BF16)
    return lax.dot_general(a, b, (((1,), (1,)), ((), ())), precision=prec, preferred_element_type=F32)


def _dot_tn(a, b, prec=None):
    if prec is None:
        a, b = a.astype(BF16), b.astype(BF16)
    return lax.dot_general(a, b, (((0,), (0,)), ((), ())), precision=prec, preferred_element_type=F32)


def _gdr_chunk_terms(k, beta, g):
    c = GDR_CHUNK
    row = lax.broadcasted_iota(jnp.int32, (c, c), 0)
    col = lax.broadcasted_iota(jnp.int32, (c, c), 1)
    causal, strict = row >= col, row > col
    gcum = _dot(causal.astype(F32), jnp.broadcast_to(g, (c, c)), HIGHEST)
    diff = gcum - gcum.T
    decay = jnp.where(causal, jnp.exp(jnp.where(causal, diff, 0.0)), 0.0)
    kb = k * beta
    kk = _dot_nt(kb, k)
    return row, col, causal, strict, gcum, decay, kb, kk


def _unit_lower_inverse(a):
    c = a.shape[0]
    eye = (lax.broadcasted_iota(jnp.int32, (c, c), 0) == lax.broadcasted_iota(jnp.int32, (c, c), 1)).astype(F32)
    p = -a
    inv = eye + p
    step = 1
    while 2 * step < c:
        p = _dot(p, p, HIGH)
        inv = inv + _dot(inv, p, HIGH)
        step *= 2
    return inv


def _gdr_fwd(q, k, v, gb, name, comm=None):
    t = q.shape[0]
    c = GDR_CHUNK
    n = t // c

    def body(q_ref, k_ref, v_ref, gb_ref, o_ref, tm_ref, s_ref, state):
        @pl.when(pl.program_id(1) == 0)
        def _():
            state[...] = jnp.zeros_like(state)

        qv, kv, vv = q_ref[...], k_ref[...], v_ref[...]
        beta, g = gb_ref[0, :, 0:1], gb_ref[0, :, 1:2]
        row, col, causal, strict, gcum, decay, kb, kk = _gdr_chunk_terms(kv, beta, g)
        tm = _unit_lower_inverse(jnp.where(strict, kk * decay, 0.0))
        e = jnp.exp(gcum)
        u = _dot(tm, vv * beta, HIGH)
        w = _dot(tm, kb * e, HIGH)
        p = jnp.where(causal, _dot_nt(qv, kv) * decay, 0.0)
        s = state[...]
        s_ref[0, 0] = s
        tm_ref[0, 0] = tm
        vn = u - _dot(w, s)
        o_ref[...] = _dot(qv * e, s) + _dot(p, vn)
        glast = gcum[c - 1:c, :]
        state[...] = s * jnp.exp(glast) + _dot_tn(kv * jnp.exp(glast - gcum), vn)

    blk = pl.BlockSpec((c, HEAD_A), lambda h, i: (i, h))
    mat = pl.BlockSpec((1, 1, c, c), lambda h, i: (h, i, 0, 0))
    return _call(
        body, name=name, grid=(N_HEADS_A, n),
        in_specs=[blk, blk, blk, pl.BlockSpec((1, c, 2), lambda h, i: (h, i, 0))],
        out_specs=[blk, mat, mat],
        out_shape=[jax.ShapeDtypeStruct((t, WIDTH_A), F32), jax.ShapeDtypeStruct((N_HEADS_A, n, c, c), F32),
                   jax.ShapeDtypeStruct((N_HEADS_A, n, HEAD_A, HEAD_A), F32)],
        scratch_shapes=[pltpu.VMEM((HEAD_A, HEAD_A), F32)], sem=("parallel", "arbitrary"),
        args=(q, k, v, gb), comm=comm)


def _gdr_bwd(q, k, v, gb, tm_all, s_all, do, name, comm=None):
    t = q.shape[0]
    c = GDR_CHUNK
    n = t // c

    def body(q_ref, k_ref, v_ref, gb_ref, tm_ref, s_ref, do_ref, dq_ref, dk_ref, dv_ref, dgb_ref, dstate):
        @pl.when(pl.program_id(1) == 0)
        def _():
            dstate[...] = jnp.zeros_like(dstate)

        qv, kv, vv, dov = q_ref[...], k_ref[...], v_ref[...], do_ref[...]
        beta, g = gb_ref[0, :, 0:1], gb_ref[0, :, 1:2]
        tm, s, dsp = tm_ref[0, 0], s_ref[0, 0], dstate[...]
        row, col, causal, strict, gcum, decay, kb, kk = _gdr_chunk_terms(kv, beta, g)
        e = jnp.exp(gcum)
        vb, kbe = vv * beta, kb * e
        u = _dot(tm, vb, HIGH)
        w = _dot(tm, kbe, HIGH)
        qk = _dot_nt(qv, kv)
        p = jnp.where(causal, qk * decay, 0.0)
        vn = u - _dot(w, s)
        glast = gcum[c - 1:c, :]
        el = jnp.exp(glast)
        f = jnp.exp(glast - gcum)
        kd = kv * f
        qe = qv * e

        dvn = _dot_tn(p, dov) + _dot(kd, dsp)
        dglast = el[:, 0:1] * jnp.sum(s * dsp, keepdims=True)
        dkd = _dot_nt(vn, dsp)
        dk = dkd * f
        df = jnp.sum(dkd * kv, axis=1, keepdims=True) * f[:, 0:1]
        dglast = dglast + jnp.sum(df, keepdims=True)
        dgc = -df
        dp = jnp.where(causal, _dot_nt(dov, vn), 0.0)
        dqe = _dot_nt(dov, s)
        dq = dqe * e
        de = jnp.sum(dqe * qv, axis=1, keepdims=True)
        dstate[...] = dsp * el + _dot_tn(qe, dov) - _dot_tn(w, dvn)
        dw = -_dot_nt(dvn, s)
        dvb = _dot_tn(tm, dvn, HIGH)
        dkbe = _dot_tn(tm, dw, HIGH)
        da = -jnp.where(strict, _dot_nt(dvb, u) + _dot_nt(dkbe, w), 0.0)
        dkk = da * decay
        dqk = dp * decay
        dd = da * kk + dp * qk
        dq = dq + _dot(dqk, kv)
        dk = dk + _dot_tn(dqk, qv)
        dkb = _dot(dkk, kv) + dkbe * e
        dk = dk + _dot_tn(dkk, kb)
        de = de + jnp.sum(dkbe * kb, axis=1, keepdims=True)
        dk = dk + dkb * beta
        dbeta = jnp.sum(dkb * kv, axis=1, keepdims=True) + jnp.sum(dvb * vv, axis=1, keepdims=True)
        m = dd * decay
        dgc = dgc + jnp.sum(m, axis=1, keepdims=True) - jnp.sum(m.T, axis=1, keepdims=True)
        dgc = dgc + de * e[:, 0:1]
        dgc = dgc + jnp.where(row[:, 0:1] == c - 1, dglast, 0.0)
        dg = _dot((row <= col).astype(F32), jnp.broadcast_to(dgc, (c, c)), HIGHEST)
        dq_ref[...] = dq
        dk_ref[...] = dk
        dv_ref[...] = dvb * beta
        dgb_ref[0, :, 0:1] = dbeta
        dgb_ref[0, :, 1:2] = dg[:, 0:1]

    blk = pl.BlockSpec((c, HEAD_A), lambda h, i: (n - 1 - i, h))
    mat = pl.BlockSpec((1, 1, c, c), lambda h, i: (h, n - 1 - i, 0, 0))
    gsp = pl.BlockSpec((1, c, 2), lambda h, i: (h, n - 1 - i, 0))
    return _call(
        body, name=name, grid=(N_HEADS_A, n),
        in_specs=[blk, blk, blk, gsp, mat, mat, blk], out_specs=[blk, blk, blk, gsp],
        out_shape=[jax.ShapeDtypeStruct((t, WIDTH_A), F32)] * 3 + [jax.ShapeDtypeStruct((N_HEADS_A, t, 2), F32)],
        scratch_shapes=[pltpu.VMEM((HEAD_A, HEAD_A), F32)], sem=("parallel", "arbitrary"),
        args=(q, k, v, gb, tm_all, s_all, do), comm=comm)


def _onorm_fwd(o, gate, g, name):
    t = o.shape[0]

    def body(o_ref, gate_ref, g_ref, y_ref):
        ov, gv = o_ref[...], gate_ref[...]
        r = lax.rsqrt(jnp.mean(ov * ov, axis=-1, keepdims=True) + RMS_EPS)
        y_ref[...] = (ov * r * g_ref[...] * gv * _sigmoid(gv)).astype(BF16)

    blk = pl.BlockSpec((t, HEAD_A), lambda j: (0, j))
    return pl.pallas_call(
        body, name=name, grid=(N_HEADS_A,), in_specs=[blk, blk, pl.BlockSpec((1, HEAD_A), lambda j: (0, 0))],
        out_specs=blk, out_shape=jax.ShapeDtypeStruct((t, WIDTH_A), BF16),
        compiler_params=_params(("parallel",)))(o, gate, g)


def _onorm_bwd(o, gate, g, dy, name):
    t = o.shape[0]

    def body(o_ref, gate_ref, g_ref, dy_ref, do_ref, dgate_ref, dg_ref):
        @pl.when(pl.program_id(0) == 0)
        def _():
            dg_ref[...] = jnp.zeros_like(dg_ref)

        ov, gv, dyv = o_ref[...], gate_ref[...], dy_ref[...].astype(F32)
        r = lax.rsqrt(jnp.mean(ov * ov, axis=-1, keepdims=True) + RMS_EPS)
        oh = ov * r
        sg, dsg = _silu_and_grad(gv)
        dgate_ref[...] = (dyv * oh * g_ref[...] * dsg).astype(BF16)
        dn = dyv * sg
        dg_ref[...] += jnp.sum(dn * oh, axis=0, keepdims=True)
        dng = dn * g_ref[...]
        do_ref[...] = r * (dng - oh * jnp.mean(dng * oh, axis=-1, keepdims=True))

    blk = pl.BlockSpec((t, HEAD_A), lambda j: (0, j))
    vec = pl.BlockSpec((1, HEAD_A), lambda j: (0, 0))
    return pl.pallas_call(
        body, name=name, grid=(N_HEADS_A,), in_specs=[blk, blk, vec, blk], out_specs=[blk, blk, vec],
        out_shape=[jax.ShapeDtypeStruct((t, WIDTH_A), F32), jax.ShapeDtypeStruct((t, WIDTH_A), BF16),
                   jax.ShapeDtypeStruct((1, HEAD_A), F32)],
        compiler_params=_params(("arbitrary",)))(o, gate, g, dy)


def _cmul(ar, ai, br, bi):
    return ar * br - ai * bi, ar * bi + ai * br


def _scan_tables(ar, ai, reverse):
    p1 = (ar, ai)
    p2 = _cmul(*p1, *p1)
    p4 = _cmul(*p2, *p2)
    p8 = _cmul(*p4, *p4)
    p3 = _cmul(*p2, *p1)
    p5 = _cmul(*p4, *p1)
    p6 = _cmul(*p4, *p2)
    p7 = _cmul(*p4, *p3)
    pows = [p1, p2, p3, p4, p5, p6, p7, p8]
    rows = lax.broadcasted_iota(jnp.int32, (8, ar.shape[1]), 0)
    tr = jnp.zeros((8, ar.shape[1]), F32)
    ti = jnp.zeros((8, ar.shape[1]), F32)
    for r in range(8):
        pw = pows[7 - r] if reverse else pows[r]
        tr = jnp.where(rows == r, pw[0], tr)
        ti = jnp.where(rows == r, pw[1], ti)
    return p1, p2, p4, p8, tr, ti


def _tile_scan(xr, xi, p1, p2, p4, reverse):
    rows = lax.broadcasted_iota(jnp.int32, xr.shape, 0)
    for s, (pr, pi) in ((1, p1), (2, p2), (4, p4)):
        if reverse:
            keep = rows < 8 - s
            sr, si = pltpu.roll(xr, 8 - s, 0), pltpu.roll(xi, 8 - s, 0)
        else:
            keep = rows >= s
            sr, si = pltpu.roll(xr, s, 0), pltpu.roll(xi, s, 0)
        sr, si = jnp.where(keep, sr, 0.0), jnp.where(keep, si, 0.0)
        mr, mi = _cmul(pr, pi, sr, si)
        xr, xi = xr + mr, xi + mi
    return xr, xi


def _s5_scan_fwd(bu, a, name, tb=512, comm=None):
    t = bu.shape[0]
    cb = SCAN_CB
    nt = t // tb

    def body(b_ref, a_ref, x_ref, carry):
        @pl.when(pl.program_id(1) == 0)
        def _():
            carry[...] = jnp.zeros_like(carry)

        ar, ai = a_ref[:, 0:cb], a_ref[:, cb:2 * cb]
        p1, p2, p4, p8, tr, ti = _scan_tables(ar, ai, False)

        def step(j, c):
            cr, ci = c
            i = pl.multiple_of(j * 8, 8)
            xr, xi = _tile_scan(b_ref[pl.ds(i, 8), 0:cb], b_ref[pl.ds(i, 8), cb:2 * cb], p1, p2, p4, False)
            mr, mi = _cmul(tr, ti, cr, ci)
            xr, xi = xr + mr, xi + mi
            x_ref[pl.ds(i, 8), 0:cb] = xr
            x_ref[pl.ds(i, 8), cb:2 * cb] = xi
            return xr[7:8, :], xi[7:8, :]

        cr, ci = lax.fori_loop(0, tb // 8, step, (carry[0:1, :], carry[1:2, :]), unroll=2)
        carry[0:1, :] = cr
        carry[1:2, :] = ci

    blk = pl.BlockSpec((tb, 2 * cb), lambda j, i: (i, j))
    return _call(
        body, name=name, grid=(SSM_CH // cb, nt),
        in_specs=[blk, pl.BlockSpec((1, 2 * cb), lambda j, i: (0, j))], out_specs=blk,
        out_shape=jax.ShapeDtypeStruct((t, 2 * SSM_CH), F32), scratch_shapes=[pltpu.VMEM((8, cb), F32)],
        sem=("parallel", "arbitrary"), args=(bu, a), comm=comm)


def _s5_scan_bwd(dx, x, a, name, tb=512, comm=None):
    t = dx.shape[0]
    cb = SCAN_CB
    nt = t // tb
    nj = tb // 8

    def body(d_ref, x_ref, xp_ref, a_ref, l_ref, da_ref, carry, acc):
        tblk = pl.program_id(1)

        @pl.when(tblk == 0)
        def _():
            carry[...] = jnp.zeros_like(carry)
            acc[...] = jnp.zeros_like(acc)

        ar, ai = a_ref[:, 0:cb], a_ref[:, cb:2 * cb]
        p1, p2, p4, p8, tr, ti = _scan_tables(ar, -ai, True)
        rows = lax.broadcasted_iota(jnp.int32, (8, cb), 0)

        def step(jj, c):
            cr, ci, sr_acc, si_acc = c
            j = nj - 1 - jj
            i = pl.multiple_of(j * 8, 8)
            lr, li = _tile_scan(d_ref[pl.ds(i, 8), 0:cb], d_ref[pl.ds(i, 8), cb:2 * cb], p1, p2, p4, True)
            mr, mi = _cmul(tr, ti, cr, ci)
            lr, li = lr + mr, li + mi
            l_ref[pl.ds(i, 8), 0:cb] = lr
            l_ref[pl.ds(i, 8), cb:2 * cb] = li
            ip = pl.multiple_of(jnp.maximum(j - 1, 0) * 8, 8)
            prev_r = jnp.where(j > 0, x_ref[pl.ds(ip, 8), 0:cb], xp_ref[:, 0:cb])
            prev_i = jnp.where(j > 0, x_ref[pl.ds(ip, 8), cb:2 * cb], xp_ref[:, cb:2 * cb])
            edge = jnp.where(jnp.logical_and(j == 0, tblk == nt - 1), 0.0, 1.0)
            xs_r = jnp.where(rows == 0, pltpu.roll(prev_r, 1, 0) * edge, pltpu.roll(x_ref[pl.ds(i, 8), 0:cb], 1, 0))
            xs_i = jnp.where(rows == 0, pltpu.roll(prev_i, 1, 0) * edge, pltpu.roll(x_ref[pl.ds(i, 8), cb:2 * cb], 1, 0))
            sr_acc = sr_acc + lr * xs_r + li * xs_i
            si_acc = si_acc + li * xs_r - lr * xs_i
            return lr[0:1, :], li[0:1, :], sr_acc, si_acc

        cr, ci, sr_acc, si_acc = lax.fori_loop(
            0, nj, step, (carry[0:1, :], carry[1:2, :], acc[:, 0:cb], acc[:, cb:2 * cb]))
        carry[0:1, :] = cr
        carry[1:2, :] = ci
        acc[:, 0:cb] = sr_acc
        acc[:, cb:2 * cb] = si_acc

        @pl.when(tblk == nt - 1)
        def _():
            da_ref[...] = jnp.sum(acc[...], axis=0, keepdims=True)

    blk = pl.BlockSpec((tb, 2 * cb), lambda j, i: (nt - 1 - i, j))
    prev = pl.BlockSpec((8, 2 * cb), lambda j, i: (jnp.maximum((nt - 1 - i) * (tb // 8) - 1, 0), j))
    vec = pl.BlockSpec((1, 2 * cb), lambda j, i: (0, j))
    return _call(
        body, name=name, grid=(SSM_CH // cb, nt), in_specs=[blk, blk, prev, vec], out_specs=[blk, vec],
        out_shape=[jax.ShapeDtypeStruct((t, 2 * SSM_CH), F32), jax.ShapeDtypeStruct((1, 2 * SSM_CH), F32)],
        scratch_shapes=[pltpu.VMEM((8, cb), F32), pltpu.VMEM((8, 2 * cb), F32)],
        sem=("parallel", "arbitrary"), args=(dx, x, x, a), comm=comm)


def _glu_fwd(yc, u, dvec, wg, bg, name, tr=256):
    t = yc.shape[0]

    def body(yc_ref, u_ref, d_ref, w_ref, b_ref, yl_ref, yb_ref):
        yl = yc_ref[...] + d_ref[...] * u_ref[...]
        yl_ref[...] = yl
        yg, _ = _gelu_and_grad(yl)
        z = jnp.dot(yg.astype(BF16), w_ref[...], preferred_element_type=F32) + b_ref[...]
        yb_ref[...] = (yg * _sigmoid(z)).astype(BF16)

    blk = pl.BlockSpec((tr, SSM_WIDTH), lambda i: (i, 0))
    vec = pl.BlockSpec((1, SSM_WIDTH), lambda i: (0, 0))
    return pl.pallas_call(
        body, name=name, grid=(t // tr,),
        in_specs=[blk, blk, vec, pl.BlockSpec((SSM_WIDTH, SSM_WIDTH), lambda i: (0, 0)), vec],
        out_specs=[blk, blk],
        out_shape=[jax.ShapeDtypeStruct((t, SSM_WIDTH), F32), jax.ShapeDtypeStruct((t, SSM_WIDTH), BF16)],
        compiler_params=_params(("parallel",)))(yc, u, dvec, wg, bg)


def _glu_bwd(yl, u, dvec, wg, bg, dyb, name, tr=256):
    t = yl.shape[0]

    def body(yl_ref, u_ref, d_ref, w_ref, b_ref, dy_ref, dyl_ref, du_ref, dw_ref, db_ref, dd_ref):
        @pl.when(pl.program_id(0) == 0)
        def _():
            dw_ref[...] = jnp.zeros_like(dw_ref)
            db_ref[...] = jnp.zeros_like(db_ref)
            dd_ref[...] = jnp.zeros_like(dd_ref)

        ylv, dyv, wv = yl_ref[...], dy_ref[...].astype(F32), w_ref[...]
        yg, dgelu = _gelu_and_grad(ylv)
        ygb = yg.astype(BF16)
        z = jnp.dot(ygb, wv, preferred_element_type=F32) + b_ref[...]
        sg = _sigmoid(z)
        dz = dyv * yg * sg * (1.0 - sg)
        dzb = dz.astype(BF16)
        dyg = dyv * sg + lax.dot_general(dzb, wv, (((1,), (1,)), ((), ())), preferred_element_type=F32)
        dyl = dyg * dgelu
        dyl_ref[...] = dyl.astype(BF16)
        du_ref[...] = dyl * d_ref[...]
        dw_ref[...] += lax.dot_general(ygb, dzb, (((0,), (0,)), ((), ())), preferred_element_type=F32)
        db_ref[...] += jnp.sum(dz, axis=0, keepdims=True)
        dd_ref[...] += jnp.sum(dyl * u_ref[...], axis=0, keepdims=True)

    blk = pl.BlockSpec((tr, SSM_WIDTH), lambda i: (i, 0))
    vec = pl.BlockSpec((1, SSM_WIDTH), lambda i: (0, 0))
    wsp = pl.BlockSpec((SSM_WIDTH, SSM_WIDTH), lambda i: (0, 0))
    return pl.pallas_call(
        body, name=name, grid=(t // tr,), in_specs=[blk, blk, vec, wsp, vec, blk],
        out_specs=[blk, blk, wsp, vec, vec],
        out_shape=[jax.ShapeDtypeStruct((t, SSM_WIDTH), BF16), jax.ShapeDtypeStruct((t, SSM_WIDTH), F32),
                   jax.ShapeDtypeStruct((SSM_WIDTH, SSM_WIDTH), F32), jax.ShapeDtypeStruct((1, SSM_WIDTH), F32),
                   jax.ShapeDtypeStruct((1, SSM_WIDTH), F32)],
        compiler_params=_params(("arbitrary",)))(yl, u, dvec, wg, bg, dyb)


def _mesh_pos():
    return lax.axis_index("x"), lax.axis_index("y"), lax.axis_index("c")


def _device_index():
    x, y, c = _mesh_pos()
    return 4 * x + 2 * y + c


def _gather_comm(arrays):
    na = len(arrays)

    def ctx(ins, outs, sems):
        send_sems, recv_sems = sems
        x, y, c = _mesh_pos()
        chips = [(1 - x, y), (x, 1 - y), (1 - x, 1 - y)]

        def copy(ai, kk, block, to, own=False):
            slot = outs[ai].at[4 * block[0] + 2 * block[1] + block[2]]
            return pltpu.make_async_remote_copy(
                src_ref=ins[ai] if own else slot, dst_ref=slot, send_sem=send_sems.at[ai, kk],
                recv_sem=recv_sems.at[ai, kk], device_id=to, device_id_type=MESH)

        return (x, y, c), (x, y, 1 - c), chips, c, copy

    def start(ins, outs, sems):
        me, sibling, chips, c, copy = ctx(ins, outs, sems)
        for ai in range(na):
            copy(ai, 0, me, sibling, own=True).start()
            for j, chip in enumerate(chips):
                copy(ai, 1 + j, me, (*chip, c), own=True).start()

    def mid(ins, outs, sems):
        me, sibling, chips, c, copy = ctx(ins, outs, sems)
        for ai in range(na):
            for j, chip in enumerate(chips):
                copy(ai, 1 + j, (*chip, c), me).wait_recv()
                copy(ai, 4 + j, (*chip, c), sibling).start()

    def end(ins, outs, sems):
        me, sibling, chips, c, copy = ctx(ins, outs, sems)
        for ai in range(na):
            copy(ai, 0, sibling, me).wait_recv()
            copy(ai, 0, me, sibling, own=True).wait_send()
            for j, chip in enumerate(chips):
                copy(ai, 4 + j, (*chip, 1 - c), me).wait_recv()
                copy(ai, 1 + j, me, (*chip, c), own=True).wait_send()
                copy(ai, 4 + j, (*chip, c), sibling).wait_send()

    return Comm(arrays, [jax.ShapeDtypeStruct((N_DEV,) + a.shape, a.dtype) for a in arrays],
                [pltpu.SemaphoreType.DMA((na, 7)), pltpu.SemaphoreType.DMA((na, 7))], start, end, mid)


def _fill_own(gathered, local):
    return lax.dynamic_update_index_in_dim(gathered, local, _device_index(), 0)


def _swap_comm(arrays):
    na = len(arrays)
    offs = np.concatenate([[0], np.cumsum([a.shape[1] for a in arrays])]).astype(int)

    def copies(ins, outs, sems):
        x, y, c = _mesh_pos()
        return [pltpu.make_async_remote_copy(
            src_ref=ins[ai].at[2 * k + 1 - c], dst_ref=outs[0].at[k, pl.ds(int(offs[ai]), arrays[ai].shape[1])],
            send_sem=sems[0].at[ai, k], recv_sem=sems[1].at[ai, k], device_id=(x, y, 1 - c), device_id_type=MESH)
            for ai in range(na) for k in range(4)]

    def start(ins, outs, sems):
        for cp in copies(ins, outs, sems):
            cp.start()

    def end(ins, outs, sems):
        for cp in copies(ins, outs, sems):
            cp.wait()

    return Comm(arrays, [jax.ShapeDtypeStruct((4, int(offs[-1]), PACK_COLS), arrays[0].dtype)],
                [pltpu.SemaphoreType.DMA((na, 4)), pltpu.SemaphoreType.DMA((na, 4))], start, end)


def _chips_comm(send):
    def copies(ins, outs, sems):
        x, y, c = _mesh_pos()
        chips = [(1 - x, y), (x, 1 - y), (1 - x, 1 - y)]
        return [pltpu.make_async_remote_copy(
            src_ref=ins[0].at[2 * cx + cy], dst_ref=outs[0].at[j], send_sem=sems[0].at[j], recv_sem=sems[1].at[j],
            device_id=(cx, cy, c), device_id_type=MESH) for j, (cx, cy) in enumerate(chips)]

    def start(ins, outs, sems):
        for cp in copies(ins, outs, sems):
            cp.start()

    def end(ins, outs, sems):
        for cp in copies(ins, outs, sems):
            cp.wait()

    return Comm([send], [jax.ShapeDtypeStruct((3,) + send.shape[1:], send.dtype)],
                [pltpu.SemaphoreType.DMA((3,)), pltpu.SemaphoreType.DMA((3,))], start, end)


def _all_gather(arrays, name):
    na = len(arrays)

    def body(*refs):
        ins, outs = refs[:na], refs[na:2 * na]
        send_sems, recv_sems, local_sems = refs[2 * na:]
        x, y, c = _mesh_pos()
        me, sibling = (x, y, c), (x, y, 1 - c)
        chips = [(1 - x, y), (x, 1 - y), (1 - x, 1 - y)]
        waits = []
        for ai in range(na):
            in_ref, out_ref = ins[ai], outs[ai]

            def slot(px, py, pc, out_ref=out_ref):
                return out_ref.at[4 * px + 2 * py + pc]

            def copy(kk, block, to, src=None, ai=ai, slot=slot):
                return pltpu.make_async_remote_copy(
                    src_ref=slot(*block) if src is None else src, dst_ref=slot(*block),
                    send_sem=send_sems.at[ai, kk], recv_sem=recv_sems.at[ai, kk], device_id=to, device_id_type=MESH)

            mine = pltpu.make_async_copy(in_ref, slot(*me), local_sems.at[ai])
            mine.start()
            first = [copy(0, me, sibling, src=in_ref)]
            first += [copy(1 + j, me, (*chip, c), src=in_ref) for j, chip in enumerate(chips)]
            for cp in first:
                cp.start()
            waits.append((copy, mine, first))
        sends = []
        for ai in range(na):
            copy, mine, first = waits[ai]
            passed = [copy(4 + j, (*chip, c), sibling) for j, chip in enumerate(chips)]
            for j, chip in enumerate(chips):
                copy(1 + j, (*chip, c), me).wait_recv()
                passed[j].start()
            sends.append(passed)
        for ai in range(na):
            copy, mine, first = waits[ai]
            copy(0, sibling, me).wait_recv()
            for j, chip in enumerate(chips):
                copy(4 + j, (*chip, 1 - c), me).wait_recv()
            for cp in first + sends[ai]:
                cp.wait_send()
            mine.wait()

    any_spec = pl.BlockSpec(memory_space=pl.ANY)
    return pl.pallas_call(
        body, name=name, in_specs=[any_spec] * na, out_specs=[any_spec] * na,
        out_shape=[jax.ShapeDtypeStruct((N_DEV,) + a.shape, a.dtype) for a in arrays],
        scratch_shapes=[pltpu.SemaphoreType.DMA((na, 7)), pltpu.SemaphoreType.DMA((na, 7)),
                        pltpu.SemaphoreType.DMA((na,))],
        compiler_params=pltpu.CompilerParams(has_side_effects=True))(*arrays)


def _swap_sibling(arrays, name):
    na = len(arrays)
    offs = np.concatenate([[0], np.cumsum([a.shape[1] for a in arrays])]).astype(int)
    rows = int(offs[-1])

    def body(*refs):
        ins, recv_ref = refs[:na], refs[na]
        send_sems, recv_sems = refs[na + 1:]
        x, y, c = _mesh_pos()
        started = []
        for ai in range(na):
            span = pl.ds(int(offs[ai]), arrays[ai].shape[1])
            for k in range(4):
                remote = pltpu.make_async_remote_copy(
                    src_ref=ins[ai].at[2 * k + 1 - c], dst_ref=recv_ref.at[k, span], send_sem=send_sems.at[ai, k],
                    recv_sem=recv_sems.at[ai, k], device_id=(x, y, 1 - c), device_id_type=MESH)
                remote.start()
                started.append(remote)
        for remote in started:
            remote.wait()

    any_spec = pl.BlockSpec(memory_space=pl.ANY)
    return pl.pallas_call(
        body, name=name, in_specs=[any_spec] * na, out_specs=any_spec,
        out_shape=jax.ShapeDtypeStruct((4, rows, PACK_COLS), arrays[0].dtype),
        scratch_shapes=[pltpu.SemaphoreType.DMA((na, 4)), pltpu.SemaphoreType.DMA((na, 4))])(*arrays)


def _exchange_chips(send, name):
    def body(s_ref, o_ref, send_sems, recv_sems):
        x, y, c = _mesh_pos()
        chips = [(1 - x, y), (x, 1 - y), (1 - x, 1 - y)]
        cps = [pltpu.make_async_remote_copy(
            src_ref=s_ref.at[2 * cx + cy], dst_ref=o_ref.at[j], send_sem=send_sems.at[j], recv_sem=recv_sems.at[j],
            device_id=(cx, cy, c), device_id_type=MESH) for j, (cx, cy) in enumerate(chips)]
        for cp in cps:
            cp.start()
        for cp in cps:
            cp.wait()

    any_spec = pl.BlockSpec(memory_space=pl.ANY)
    return pl.pallas_call(
        body, name=name, in_specs=[any_spec], out_specs=any_spec,
        out_shape=jax.ShapeDtypeStruct((3,) + send.shape[1:], send.dtype),
        scratch_shapes=[pltpu.SemaphoreType.DMA((3,)), pltpu.SemaphoreType.DMA((3,))])(send)


def _pair_sum(keep, recv, name, tr=464):
    nchip, rows, cols = keep.shape

    def body(g_ref, r_ref, o_ref):
        o_ref[...] = (g_ref[...].astype(F32) + r_ref[...].astype(F32)).astype(BF16)

    blk = pl.BlockSpec((1, tr, cols), lambda k, i: (k, i, 0))
    return pl.pallas_call(
        body, name=name, grid=(nchip, rows // tr), in_specs=[blk, blk], out_specs=blk,
        out_shape=jax.ShapeDtypeStruct((nchip, rows, cols), BF16),
        compiler_params=_params(("parallel", "parallel")))(keep, recv)


def _chip_sum(own, others, name, tr=464):
    _, rows, cols = own.shape
    chip = (2 * lax.axis_index("x") + lax.axis_index("y")).astype(jnp.int32).reshape(1)

    def body(chip_ref, own_ref, oth_ref, o_ref):
        del chip_ref
        acc = own_ref[0].astype(F32)
        for j in range(3):
            acc = acc + oth_ref[j].astype(F32)
        o_ref[...] = acc

    grid_spec = pltpu.PrefetchScalarGridSpec(
        num_scalar_prefetch=1, grid=(rows // tr,),
        in_specs=[pl.BlockSpec((1, tr, cols), lambda i, chip_ref: (chip_ref[0], i, 0)),
                  pl.BlockSpec((3, tr, cols), lambda i, chip_ref: (0, i, 0))],
        out_specs=pl.BlockSpec((tr, cols), lambda i, chip_ref: (i, 0)))
    return pl.pallas_call(
        body, name=name, grid_spec=grid_spec, out_shape=jax.ShapeDtypeStruct((rows, cols), F32),
        compiler_params=_params(("parallel",)))(chip, own, others)


def _sum_leading(parts, name, tr=464):
    nparts, rows, cols = parts.shape
    tr = tr if rows % tr == 0 else rows

    def body(p_ref, o_ref):
        acc = p_ref[0].astype(F32)
        for i in range(1, nparts):
            acc = acc + p_ref[i].astype(F32)
        o_ref[...] = acc

    return pl.pallas_call(
        body, name=name, grid=(rows // tr,),
        in_specs=[pl.BlockSpec((nparts, tr, cols), lambda i: (0, i, 0))],
        out_specs=pl.BlockSpec((tr, cols), lambda i: (i, 0)), out_shape=jax.ShapeDtypeStruct((rows, cols), F32),
        compiler_params=_params(("parallel",)))(parts)


def _adamw(w, g, m, v, name):
    shape = w.shape
    cols = shape[-1]
    rows = int(np.prod(shape[:-1])) if len(shape) > 1 else 1
    w2, g2, m2, v2 = (a.reshape(rows, cols) for a in (w, g, m, v))
    tr = rows
    for cand in (512, 256, 128, 64, 32, 16, 8):
        if rows % cand == 0 and rows > cand:
            tr = cand
            break
    bc1, bc2 = 1.0 - ADAM_B1 ** ADAM_STEP, 1.0 - ADAM_B2 ** ADAM_STEP

    def body(w_ref, g_ref, m_ref, v_ref, d_ref, nm_ref, nv_ref):
        gv = g_ref[...]
        nm = ADAM_B1 * m_ref[...] + (1.0 - ADAM_B1) * gv
        nv = ADAM_B2 * v_ref[...] + (1.0 - ADAM_B2) * (gv * gv)
        nm_ref[...] = nm
        nv_ref[...] = nv
        d_ref[...] = -ADAM_LR * ((nm / bc1) / (jnp.sqrt(nv / bc2) + ADAM_EPS) + ADAM_WD * w_ref[...])

    blk = pl.BlockSpec((tr, cols), lambda i: (i, 0))
    outs = pl.pallas_call(
        body, name=name, grid=(rows // tr,), in_specs=[blk] * 4, out_specs=[blk] * 3,
        out_shape=[jax.ShapeDtypeStruct((rows, cols), F32)] * 3, compiler_params=_params(("parallel",)))(w2, g2, m2, v2)
    return tuple(o.reshape(shape) for o in outs)


WEIGHT_NAMES = ['norm_mix_g', 'norm_xa_g', 'norm_ffn_g', 'norm_mem_g', 'norm_final_g', 'w_in_ab', 'conv_qkv_a',
                'a_log_a', 'dt_bias_a', 'onorm_g_a', 'ssm_lambda_re', 'ssm_lambda_im', 'ssm_b_re', 'ssm_b_im',
                'ssm_c_re', 'ssm_c_im', 'ssm_d', 'ssm_log_dt', 'w_glu_b', 'b_glu_b', 'w_out_ab', 'pool_w',
                'pool_scale', 'xa_wq', 'xa_wkv', 'xa_wo', 'ffn_w_up', 'ffn_conv', 'ffn_w_down']
BIG_SHARDED = {'w_in_ab': ((1, 1024, 2568), 2), 'w_glu_b': ((1, 512, 512), 1), 'w_out_ab': ((1, 1024, 1024), 1),
               'pool_w': ((1, 4, 256, 256), 2), 'xa_wq': ((2, 1024, 1024), 1), 'xa_wkv': ((2, 1024, 2048), 2),
               'xa_wo': ((2, 1024, 1024), 1), 'ffn_w_up': ((2, 1024, 5632), 2), 'ffn_w_down': ((2, 2816, 1024), 1)}
SMALL_SHARDED = {'conv_qkv_a': ((1, 4, 1536), 2), 'pool_scale': ((1, 1024), 1), 'ffn_conv': ((2, 3, 5632), 2)}
REPLICATED = {'norm_mix_g': (2, 1024), 'norm_xa_g': (2, 1024), 'norm_ffn_g': (2, 1024), 'norm_mem_g': (1024,),
              'norm_final_g': (1024,), 'a_log_a': (1, 4), 'dt_bias_a': (1, 4), 'onorm_g_a': (1, 128),
              'ssm_lambda_re': (1, 32, 64), 'ssm_lambda_im': (1, 32, 64), 'ssm_b_re': (1, 32, 64, 16),
              'ssm_b_im': (1, 32, 64, 16), 'ssm_c_re': (1, 32, 16, 64), 'ssm_c_im': (1, 32, 16, 64),
              'ssm_d': (1, 32, 16), 'ssm_log_dt': (1, 32), 'b_glu_b': (1, 512)}
BIG_ROW_ALIGN = 464 * 8


def _shard_shape(shape, axis):
    return tuple(s // N_DEV if i == axis else s for i, s in enumerate(shape))


def _pad_rows(a, rows):
    return a if a.shape[0] == rows else jnp.concatenate([a, jnp.zeros((rows - a.shape[0],) + a.shape[1:], a.dtype)], 0)


def _round_up(n, m):
    return (n + m - 1) // m * m


def _pack_rows(flat_list, align):
    rows = []
    for a in flat_list:
        n = a.shape[0]
        r = _round_up(n, PACK_COLS) // PACK_COLS
        if n != r * PACK_COLS:
            a = jnp.concatenate([a, jnp.zeros((r * PACK_COLS - n,), a.dtype)])
        rows.append(a.reshape(r, PACK_COLS))
    out = jnp.concatenate(rows, 0)
    return _pad_rows(out, _round_up(out.shape[0], align))


def _row_offsets(sizes):
    offs, r = [], 0
    for n in sizes:
        offs.append(r)
        r += _round_up(n, PACK_COLS) // PACK_COLS
    return offs


def _split_shards(full, axis):
    shape = full.shape
    s = shape[axis] // N_DEV
    a = full.reshape(shape[:axis] + (N_DEV, s) + shape[axis + 1:])
    return jnp.moveaxis(a, axis, 0).reshape(N_DEV, -1)


def _merge_shards(pieces, shape, axis):
    sh = _shard_shape(shape, axis)
    a = pieces.reshape((N_DEV,) + sh)
    a = jnp.moveaxis(a, 0, axis)
    return a.reshape(shape)


_SCAN_NB = SSM_CH // SCAN_CB


def _to_scan_layout(m, axis):
    shape = m.shape
    m = m.reshape(shape[:axis] + (2, _SCAN_NB, SCAN_CB) + shape[axis + 1:])
    return jnp.swapaxes(m, axis, axis + 1).reshape(shape)


def _from_scan_layout(m, axis):
    shape = m.shape
    m = m.reshape(shape[:axis] + (_SCAN_NB, 2, SCAN_CB) + shape[axis + 1:])
    return jnp.swapaxes(m, axis, axis + 1).reshape(shape)


def _s5_discretise(lam_re, lam_im, b_re, b_im, log_dt):
    dt = jnp.exp(log_dt)[:, None]
    mag = jnp.exp(lam_re * dt)
    ang = lam_im * dt
    lb_re, lb_im = mag * jnp.cos(ang), mag * jnp.sin(ang)
    den = lam_re * lam_re + lam_im * lam_im
    nr, ni = lb_re - 1.0, lb_im
    coef_re = (nr * lam_re + ni * lam_im) / den
    coef_im = (ni * lam_re - nr * lam_im) / den
    bb_re = coef_re[..., None] * b_re - coef_im[..., None] * b_im
    bb_im = coef_re[..., None] * b_im + coef_im[..., None] * b_re
    return lb_re, lb_im, bb_re, bb_im


def _s5_matrices(lb_re, lb_im, bb_re, bb_im, c_re, c_im):
    eye = jnp.eye(N_GROUPS, dtype=F32)
    bmat = lambda bb: jnp.einsum('gph,gk->ghkp', bb, eye).reshape(SSM_WIDTH, SSM_CH)
    cmat = lambda cc: jnp.einsum('ghp,gk->gpkh', cc, eye).reshape(SSM_CH, SSM_WIDTH)
    b_in = _to_scan_layout(jnp.concatenate([bmat(bb_re), bmat(bb_im)], axis=1), 1)
    c_out = _to_scan_layout(jnp.concatenate([cmat(c_re), -cmat(c_im)], axis=0), 0)
    a_row = _to_scan_layout(jnp.concatenate([lb_re.reshape(1, SSM_CH), lb_im.reshape(1, SSM_CH)], axis=1), 1)
    return b_in, c_out, a_row


def _s5_matrix_grads(db_in, dc_out, da_row):
    db_nat = _from_scan_layout(db_in, 1)
    dc_nat = _from_scan_layout(dc_out, 0)
    da_nat = _from_scan_layout(da_row, 1)
    eye = jnp.eye(N_GROUPS, dtype=F32)
    bgrad = lambda m: jnp.einsum('ghkp,gk->gph', m.reshape(N_GROUPS, SSM_GROUP, N_GROUPS, SSM_STATE), eye)
    cgrad = lambda m: jnp.einsum('gpkh,gk->ghp', m.reshape(N_GROUPS, SSM_STATE, N_GROUPS, SSM_GROUP), eye)
    dbb_re, dbb_im = bgrad(db_nat[:, :SSM_CH]), bgrad(db_nat[:, SSM_CH:])
    dc_re, dc_im = cgrad(dc_nat[:SSM_CH]), -cgrad(dc_nat[SSM_CH:])
    dlb_re = da_nat[0, :SSM_CH].reshape(N_GROUPS, SSM_STATE)
    dlb_im = da_nat[0, SSM_CH:].reshape(N_GROUPS, SSM_STATE)
    return dlb_re, dlb_im, dbb_re, dbb_im, dc_re, dc_im


def _as_pieces(a):
    return a.reshape(N_DEV, a.shape[0] // N_DEV, a.shape[1])


def _hybrid_fwd(xn, x, wts, p, weights, riders):
    sv = {}
    hq = _mm(xn, wts['w_qkv_t'], "nt", "l0_in_qkv")
    gate = _mm(xn, wts['w_gate_t'], "nt", "l0_in_gate")
    ba = _mm(xn, wts['w_ba_t'], "nt", "l0_in_ba")
    u = _mm(xn, wts['w_u_t'], "nt", "l0_in_u")
    conv = p['conv_qkv']
    q = _qkv_pre_fwd(hq, conv, 0, 4, True, HEAD_A ** -0.5, "l0_q_pre")
    k = _qkv_pre_fwd(hq, conv, 4, 4, True, 1.0, "l0_k_pre")
    v = _qkv_pre_fwd(hq, conv, 8, 4, False, 1.0, "l0_v_pre")
    gates = _gates_fwd(ba, p['arow'], p['brow'], "l0_gates")
    gb = jnp.stack([gates[:, 0:4].T, gates[:, 4:8].T], axis=-1)
    o, tm_all, s_all = riders.run("l0_gdr_fwd", _gdr_fwd, q, k, v, gb)
    wts['w_glu'], wts['w_out'] = weights.full['w_glu'], weights.full['w_out']
    y_a = _onorm_fwd(o, gate, p['onorm_g'], "l0_onorm")
    bu = riders.run("l0_s5_bu", _mm, u, p['b_in'], "nn")
    xs = riders.run("l0_s5_scan", _s5_scan_fwd, bu, p['a_row'])
    yc = riders.run("l0_s5_cx", _mm, xs, p['c_out'], "nn")
    yl, y_b = _glu_fwd(yc, u, p['d_row'], wts['w_glu'], p['b_glu'], "l0_glu")
    mixed = jnp.concatenate([y_a, y_b], axis=1)
    x1 = _mm(mixed, wts['w_out'], "nn", "l0_out", res=x)
    sv.update(hq=hq, gate=gate, ba=ba, u=u, q=q, k=k, v=v, gb=gb, o=o, tm=tm_all, s=s_all, xs=xs, yl=yl, mixed=mixed)
    return x1, sv


def _hybrid_bwd(dx1, xn, wts, p, sv, riders):
    gr = {}
    dmixed = _mm(dx1, wts['w_out'], "nt", "l0_out_dx", out_dtype=BF16)
    riders.grad('w_out', _as_pieces(_mm(sv['mixed'], dx1, "tn", "l0_out_dw", out_dtype=BF16)))
    dya, dyb = dmixed[:, :WIDTH_A], dmixed[:, WIDTH_A:]
    dyl, du_direct, dw_glu, gr['b_glu_b'], dd = _glu_bwd(
        sv['yl'], sv['u'], p['d_row'], wts['w_glu'], p['b_glu'], dyb, "l0_glu_bwd")
    riders.grad('w_glu', dw_glu.astype(BF16).reshape(N_DEV, -1, PACK_COLS))
    dxs = riders.run("l0_s5_cx_dx", _mm, dyl, p['c_out'], "nt")
    dc_out = _mm(sv['xs'], dyl, "tn", "l0_s5_cx_dw")
    lam, da_row = riders.run("l0_s5_scan_bwd", _s5_scan_bwd, dxs, sv['xs'], p['a_row'])
    du = _mm(lam, p['b_in'], "nt", "l0_s5_bu_dx", res=du_direct, out_dtype=BF16)
    db_in = _mm(sv['u'], lam, "tn", "l0_s5_bu_dw")
    gr['s5'] = (db_in, dc_out, da_row, dd)
    do, dgate, gr['onorm_g_a'] = _onorm_bwd(sv['o'], sv['gate'], p['onorm_g'], dya, "l0_onorm_bwd")
    dq, dk, dv, dgb = riders.run("l0_gdr_bwd", _gdr_bwd, sv['q'], sv['k'], sv['v'], sv['gb'], sv['tm'], sv['s'], do)
    conv = p['conv_qkv']
    dhq_q, dcw_q = _qkv_pre_bwd(sv['hq'], conv, dq, 0, 4, True, HEAD_A ** -0.5, "l0_q_pre_bwd")
    dhq_k, dcw_k = _qkv_pre_bwd(sv['hq'], conv, dk, 4, 4, True, 1.0, "l0_k_pre_bwd")
    dhq_v, dcw_v = _qkv_pre_bwd(sv['hq'], conv, dv, 8, 4, False, 1.0, "l0_v_pre_bwd")
    gr['conv_qkv_a'] = jnp.concatenate([dcw_q, dcw_k, dcw_v], axis=1)
    dhq = jnp.concatenate([dhq_q, dhq_k, dhq_v], axis=1)
    dgates = jnp.concatenate([dgb[:, :, 0].T, dgb[:, :, 1].T, jnp.zeros((SEQ, LANE - 8), F32)], axis=1)
    dba, da_log, ddt_bias = _gates_bwd(sv['ba'], p['arow'], p['brow'], dgates, "l0_gates_bwd")
    gr['a_log_a'], gr['dt_bias_a'] = da_log[:, 4:8], ddt_bias[:, 4:8]
    dxn = _mm(dhq, wts['w_qkv_t'], "nn", "l0_in_qkv_dx")
    dxn = _mm(dgate, wts['w_gate_t'], "nn", "l0_in_gate_dx", res=dxn)
    dxn = _mm(dba, wts['w_ba_t'], "nn", "l0_in_ba_dx", res=dxn)
    dxn = _mm(du, wts['w_u_t'], "nn", "l0_in_u_dx", res=dxn)
    dw_qkv_t = _mm(dhq, xn, "tn", "l0_in_qkv_dw", out_dtype=BF16)
    dw_gate_t = _mm(dgate, xn, "tn", "l0_in_gate_dw", out_dtype=BF16)
    dw_ba_t = _mm(dba, xn, "tn", "l0_in_ba_dw", out_dtype=BF16)
    dw_u_t = _mm(du, xn, "tn", "l0_in_u_dw", out_dtype=BF16)
    dw_in_t = _as_pieces(jnp.concatenate([dw_qkv_t, dw_gate_t, dw_ba_t[:8], dw_u_t], axis=0))
    riders.grad('w_in_t', jnp.concatenate(
        [dw_in_t, jnp.zeros((N_DEV, dict(PIECES)['w_in_t'] - W_IN_PIECE, D_MODEL), BF16)], axis=1))
    return dxn, gr


def _xa_fwd(x1, g, mem_n, wq, wkv_t, wo, tag, riders):
    xq = _rms_fwd(x1, g, BF16, tag + "_norm")
    q = _mm(xq, wq, "nn", tag + "_q", out_dtype=BF16)
    kv = _mm(mem_n, wkv_t, "nt", tag + "_kv", out_dtype=BF16)
    o = riders.run(tag + "_attn", _attn_fwd, q, kv)
    x2 = _mm(o, wo, "nn", tag + "_o", res=x1)
    return x2, dict(xq=xq, q=q, kv=kv, o=o)


def _xa_bwd(dx2, x1, g, mem_n, wq, wkv_t, wo, sv, tag, layer, riders):
    do = _mm(dx2, wo, "nt", tag + "_o_dx", out_dtype=BF16)
    riders.grad('wo%d' % layer, _as_pieces(_mm(sv['o'], dx2, "tn", tag + "_o_dw", out_dtype=BF16)))
    dq, dk, dv = _attn_bwd(sv['q'], sv['kv'], do, tag + "_attn_bwd")
    dkv = jnp.concatenate([dk, dv], axis=1).astype(BF16)
    dxq = _mm(dq, wq, "nt", tag + "_q_dx")
    riders.grad('wq%d' % layer, _as_pieces(_mm(sv['xq'], dq, "tn", tag + "_q_dw", out_dtype=BF16)))
    dmem_n = _mm(dkv, wkv_t, "nn", tag + "_kv_dx")
    riders.grad('wkv_t%d' % layer, _as_pieces(_mm(dkv, mem_n, "tn", tag + "_kv_dw", out_dtype=BF16)))
    dx1, dg = riders.run(tag + "_norm_bwd", _rms_bwd, x1, g, dxq, dx2)
    return dx1, dmem_n, dg


def _ffn_fwd(x2, g, w_up_t, conv, w_down, tag, riders):
    xf = _rms_fwd(x2, g, BF16, tag + "_norm")
    h = riders.run(tag + "_up", _mm, xf, w_up_t, "nt")
    a = riders.run(tag + "_act", _ffn_act_fwd, h, conv)
    x3 = _mm(a, w_down, "nn", tag + "_down", res=x2)
    return x3, dict(xf=xf, h=h, a=a)


def _ffn_bwd(dx3, x2, g, w_up_t, conv, w_down, sv, tag, layer, riders):
    da = _mm(dx3, w_down, "nt", tag + "_down_dx")
    riders.grad('down%d' % layer, _as_pieces(_mm(sv['a'], dx3, "tn", tag + "_down_dw", out_dtype=BF16)))
    dh, dconv = riders.run(tag + "_act_bwd", _ffn_act_bwd, sv['h'], conv, da)
    dxf = riders.run(tag + "_up_dx", _mm, dh, w_up_t, "nn")
    dw_up_t = riders.run(tag + "_up_dw", _mm, dh, sv['xf'], "tn", out_dtype=BF16)
    riders.grad('up_t%d' % layer, _as_pieces(dw_up_t))
    dx2, dg = riders.run(tag + "_norm_bwd", _rms_bwd, x2, g, dxf, dx3)
    return dx2, dconv, dg


BIG_NAMES, SMALL_NAMES, REP_NAMES = list(BIG_SHARDED), list(SMALL_SHARDED), list(REPLICATED)
BIG_SIZES = [int(np.prod(_shard_shape(*BIG_SHARDED[n]))) for n in BIG_NAMES]
SMALL_SIZES = [int(np.prod(_shard_shape(*SMALL_SHARDED[n]))) for n in SMALL_NAMES]


PIECES = [('w_in_t', 384), ('w_glu', 32), ('w_out', 128), ('pool_w', 32), ('wq0', 128), ('wq1', 128),
          ('wkv_t0', 256), ('wkv_t1', 256), ('wo0', 128), ('wo1', 128), ('up_t0', 704), ('up_t1', 704),
          ('down0', 352), ('down1', 352)]
PIECE_OFFS = dict(zip([k for k, _ in PIECES], np.concatenate([[0], np.cumsum([r for _, r in PIECES])[:-1]]).tolist()))
W_IN_ROWS = 4 * WIDTH_A + 2 * N_HEADS_A + SSM_WIDTH
W_IN_PIECE = W_IN_ROWS // N_DEV


def _row_tile(rows):
    return max(t for t in range(16, min(rows, 512) + 1, 16) if rows % t == 0)


class _Riders:
    def __init__(self):
        self.waiting = {}
        self.grads = {}
        self.groups = []
        self.reduced = {}

    def add(self, host, comm, then):
        self.waiting.setdefault(host, []).append((comm, then))

    def run(self, name, fn, *args, **kw):
        riders = self.waiting.pop(name, [])
        if not riders:
            return fn(*args, name=name, **kw)
        out, couts = fn(*args, name=name, comm=[c for c, _ in riders], **kw)
        for (_, then), got in zip(riders, couts):
            then(got)
        return out

    def exchange(self, comm, host, name, then):
        if host is None:
            then(_comm_only(comm, name))
        else:
            self.add(host, comm, then)

    def grad(self, key, pieces):
        self.grads[key] = pieces
        for group in [g for g in self.groups if all(k in self.grads for k in g[1])]:
            self.groups.remove(group)
            self._reduce(*group)

    def _reduce(self, name, keys, swap_host, chips_host):
        arrays = [self.grads[k] for k in keys]
        rows = sum(a.shape[1] for a in arrays)
        tile = _row_tile(rows)

        def after_chips(chip_sums, got):
            total = _chip_sum(chip_sums, got[0], name + "_chip_sum", tr=tile)
            off = 0
            for k, a in zip(keys, arrays):
                self.reduced[k] = total[off:off + a.shape[1]]
                off += a.shape[1]

        def after_swap(got):
            core = lax.axis_index("c")
            keep = jnp.concatenate(
                [lax.dynamic_index_in_dim(a.reshape(4, 2, a.shape[1], PACK_COLS), core, 1, keepdims=False)
                 for a in arrays], axis=1)
            chip_sums = _pair_sum(keep, got[0], name + "_pair_sum", tr=tile)
            self.exchange(_chips_comm(chip_sums), chips_host, name + "_to_chips",
                          functools.partial(after_chips, chip_sums))

        self.exchange(_swap_comm(arrays), swap_host, name + "_to_sibling", after_swap)


class _Weights:
    def __init__(self, inp):
        bf = lambda a: a.astype(BF16)
        local = {'w_in_t': bf(inp['w_in_ab'][0]).T, 'w_glu': bf(inp['w_glu_b'][0]), 'w_out': bf(inp['w_out_ab'][0]),
                 'pool_w': bf(inp['pool_w'][0]),
                 'small': _pack_rows([inp[n].reshape(-1) for n in SMALL_NAMES], 8)}
        for l in range(2):
            local['wq%d' % l] = bf(inp['xa_wq'][l])
            local['wkv_t%d' % l] = bf(inp['xa_wkv'][l]).T
            local['wo%d' % l] = bf(inp['xa_wo'][l])
            local['up_t%d' % l] = bf(inp['ffn_w_up'][l]).T
            local['down%d' % l] = bf(inp['ffn_w_down'][l])
        self.local, self.full = local, {}

    def plan(self, keys):
        return _gather_comm([self.local[k] for k in keys])

    def land(self, keys, gathered):
        for k, g in zip(keys, gathered):
            g = _fill_own(g, self.local[k])
            if k == 'small':
                for n, off, size in zip(SMALL_NAMES, _row_offsets(SMALL_SIZES), SMALL_SIZES):
                    r = _round_up(size, PACK_COLS) // PACK_COLS
                    self.full[n] = _merge_shards(g[:, off:off + r].reshape(N_DEV, -1)[:, :size], *SMALL_SHARDED[n])
            elif k == 'pool_w':
                self.full[k] = jnp.swapaxes(g, 0, 1).reshape(len(POOL_WINDOWS), POOL_GROUP, POOL_GROUP)
            else:
                self.full[k] = g.reshape(N_DEV * g.shape[1], g.shape[2])


GATHER_FIRST = ['w_in_t', 'small']
GATHER_RIDES = [('l0_gdr_fwd', ['w_glu', 'w_out', 'wq0', 'wkv_t0', 'wo0', 'up_t0']),
                ('l0_s5_bu', ['pool_w', 'wq1']), ('l0_s5_scan', ['down0']), ('l0_s5_cx', ['wo1']),
                ('l0_xa_attn', ['wkv_t1']), ('l0_ffn_up', ['up_t1']), ('l0_ffn_act', ['down1'])]
GRAD_RIDES = [('g_down1', ['down1'], 'l1_ffn_act_bwd', 'l1_ffn_up_dx'),
              ('g_up1', ['up_t1'], 'l1_ffn_norm_bwd', 'l0_ffn_act_bwd'),
              ('g_xa1', ['wq1', 'wkv_t1', 'wo1', 'pool_w'], 'l1_mix_norm_bwd', 'l0_ffn_up_dx'),
              ('g_down0', ['down0'], 'l0_ffn_act_bwd', 'l0_ffn_up_dw'),
              ('g_up0', ['up_t0'], 'l0_ffn_norm_bwd', 'l0_gdr_bwd'),
              ('g_xa0', ['wq0', 'wkv_t0', 'wo0'], 'l0_xa_norm_bwd', 'l0_gdr_bwd'),
              ('g_out', ['w_out', 'w_glu'], 'l0_s5_cx_dx', 'l0_s5_scan_bwd'),
              ('g_in', ['w_in_t'], None, None)]


def _local_step(inp):
    f32_of = lambda n: inp[n].astype(F32)
    weights = _Weights(inp)
    riders = _Riders()
    riders.groups = list(GRAD_RIDES)
    full = weights.full
    weights.land(GATHER_FIRST, _comm_only(weights.plan(GATHER_FIRST), "gather_first"))
    for host, keys in GATHER_RIDES:
        riders.add(host, weights.plan(keys), functools.partial(weights.land, keys))
    w_in_t = full['w_in_t']
    wts0 = dict(w_qkv_t=w_in_t[:3 * WIDTH_A], w_gate_t=w_in_t[3 * WIDTH_A:4 * WIDTH_A],
                w_ba_t=jnp.concatenate([w_in_t[4 * WIDTH_A:4 * WIDTH_A + 8], jnp.zeros((LANE - 8, D_MODEL), BF16)], 0),
                w_u_t=w_in_t[4 * WIDTH_A + 8:])
    lb_disc, disc_vjp = jax.vjp(_s5_discretise, f32_of('ssm_lambda_re')[0], f32_of('ssm_lambda_im')[0],
                                f32_of('ssm_b_re')[0], f32_of('ssm_b_im')[0], f32_of('ssm_log_dt')[0])
    b_in, c_out, a_row = _s5_matrices(*lb_disc, f32_of('ssm_c_re')[0], f32_of('ssm_c_im')[0])
    zeros4 = jnp.zeros((1, 4), F32)
    p0 = dict(conv_qkv=full['conv_qkv_a'][0], onorm_g=f32_of('onorm_g_a'),
              arow=jnp.concatenate([zeros4, f32_of('a_log_a'), jnp.zeros((1, LANE - 8), F32)], 1),
              brow=jnp.concatenate([zeros4, f32_of('dt_bias_a'), jnp.zeros((1, LANE - 8), F32)], 1),
              b_in=b_in.astype(BF16), c_out=c_out.astype(BF16), a_row=a_row,
              d_row=f32_of('ssm_d').reshape(1, SSM_WIDTH), b_glu=f32_of('b_glu_b'))

    x0 = inp['x'][0]
    mem_n = _rms_fwd(inp['mem'][0], inp['norm_mem_g'], BF16, "mem_norm")
    xn0 = _rms_fwd(x0, inp['norm_mix_g'][0], BF16, "l0_mix_norm")
    x1, sv_mix0 = _hybrid_fwd(xn0, x0, wts0, p0, weights, riders)
    x2, sv_xa0 = _xa_fwd(x1, inp['norm_xa_g'][0], mem_n, full['wq0'], full['wkv_t0'], full['wo0'], "l0_xa", riders)
    x3, sv_ffn0 = _ffn_fwd(x2, inp['norm_ffn_g'][0], full['up_t0'], full['ffn_conv'][0], full['down0'], "l0_ffn", riders)
    xn1 = _rms_fwd(x3, inp['norm_mix_g'][1], F32, "l1_mix_norm")
    x4 = _pool_fwd(xn1, full['pool_w'], full['pool_scale'], x3, "l1_pool")
    x5, sv_xa1 = _xa_fwd(x4, inp['norm_xa_g'][1], mem_n, full['wq1'], full['wkv_t1'], full['wo1'], "l1_xa", riders)
    x6, sv_ffn1 = _ffn_fwd(x5, inp['norm_ffn_g'][1], full['up_t1'], full['ffn_conv'][1], full['down1'], "l1_ffn", riders)
    loss_part, dx6, dg_final = _loss_head(x6, inp['norm_final_g'], inp['loss_target'][0], "loss_head")

    dx5, dconv1, dg_ffn1 = _ffn_bwd(dx6, x5, inp['norm_ffn_g'][1], full['up_t1'], full['ffn_conv'][1], full['down1'],
                                    sv_ffn1, "l1_ffn", 1, riders)
    dx4, dmem1, dg_xa1 = _xa_bwd(dx5, x4, inp['norm_xa_g'][1], mem_n, full['wq1'], full['wkv_t1'], full['wo1'],
                                 sv_xa1, "l1_xa", 1, riders)
    dxn1, dpool_w, dpool_scale = _pool_bwd(xn1, full['pool_w'], full['pool_scale'], dx4, "l1_pool_bwd")
    pool_pieces = jnp.swapaxes(dpool_w.astype(BF16).reshape(len(POOL_WINDOWS), N_DEV, -1, POOL_GROUP), 0, 1)
    riders.grad('pool_w', pool_pieces.reshape(N_DEV, -1, PACK_COLS))
    dx3, dg_mix1 = riders.run("l1_mix_norm_bwd", _rms_bwd, x3, inp['norm_mix_g'][1], dxn1, dx4)
    dx2, dconv0, dg_ffn0 = _ffn_bwd(dx3, x2, inp['norm_ffn_g'][0], full['up_t0'], full['ffn_conv'][0], full['down0'],
                                    sv_ffn0, "l0_ffn", 0, riders)
    dx1, dmem0, dg_xa0 = _xa_bwd(dx2, x1, inp['norm_xa_g'][0], mem_n, full['wq0'], full['wkv_t0'], full['wo0'],
                                 sv_xa0, "l0_xa", 0, riders)
    dxn0, g_mix0 = _hybrid_bwd(dx1, xn0, wts0, p0, sv_mix0, riders)
    grad_x, dg_mix0 = _rms_bwd(x0, inp['norm_mix_g'][0], dxn0, dx1, "l0_mix_norm_bwd")
    _, dg_mem = _rms_bwd(inp['mem'][0], inp['norm_mem_g'], dmem0 + dmem1, None, "mem_norm_bwd")
    assert not riders.waiting and not riders.groups, (list(riders.waiting), riders.groups)

    db_in, dc_out, da_row, dd = g_mix0['s5']
    dlb_re, dlb_im, dbb_re, dbb_im, dc_re, dc_im = _s5_matrix_grads(db_in, dc_out, da_row)
    dlam_re, dlam_im, dbr, dbi, dlog_dt = disc_vjp((dlb_re, dlb_im, dbb_re, dbb_im))

    rep_grads = {
        'norm_mix_g': jnp.concatenate([dg_mix0, dg_mix1], 0), 'norm_xa_g': jnp.concatenate([dg_xa0, dg_xa1], 0),
        'norm_ffn_g': jnp.concatenate([dg_ffn0, dg_ffn1], 0), 'norm_mem_g': dg_mem.reshape(-1),
        'norm_final_g': dg_final.reshape(-1), 'a_log_a': g_mix0['a_log_a'], 'dt_bias_a': g_mix0['dt_bias_a'],
        'onorm_g_a': g_mix0['onorm_g_a'], 'ssm_lambda_re': dlam_re[None], 'ssm_lambda_im': dlam_im[None],
        'ssm_b_re': dbr[None], 'ssm_b_im': dbi[None], 'ssm_c_re': dc_re[None], 'ssm_c_im': dc_im[None],
        'ssm_d': dd.reshape(1, N_GROUPS, SSM_GROUP), 'ssm_log_dt': dlog_dt[None], 'b_glu_b': g_mix0['b_glu_b']}
    small_grads = {'conv_qkv_a': g_mix0['conv_qkv_a'][None], 'pool_scale': dpool_scale,
                   'ffn_conv': jnp.stack([dconv0, dconv1])}
    for key, rows in PIECES:
        assert riders.reduced[key].shape == (rows, PACK_COLS), key
    return loss_part, grad_x, riders.reduced, rep_grads, small_grads


def _reduce_small_gradients(rep_grads, small_grads):
    misc_list = [rep_grads[n].astype(F32).reshape(-1) for n in REP_NAMES] + \
                [small_grads[n].astype(F32).reshape(-1) for n in SMALL_NAMES]
    misc_sizes = [int(a.shape[0]) for a in misc_list]
    misc_local = _pack_rows(misc_list, 8)
    (misc_all,) = _all_gather([misc_local], "gather_small_grads")
    misc_sum = _sum_leading(misc_all, "small_grads_sum")
    return misc_sum, misc_sizes


def _update(inp, loss_part, grad_x, big_reduced, misc_sum, misc_sizes):
    big_names, small_names, rep_names = BIG_NAMES, SMALL_NAMES, REP_NAMES
    misc_offs = _row_offsets(misc_sizes)
    dev = 4 * lax.axis_index("x") + 2 * lax.axis_index("y") + lax.axis_index("c")
    piece = lambda key, rows=None: big_reduced[key] if rows is None else big_reduced[key][:rows]
    both = lambda name, fn: jnp.stack([fn(piece(name + '0')), fn(piece(name + '1'))])
    ident, transpose = (lambda a: a), (lambda a: a.T)
    grads = {'w_in_ab': piece('w_in_t', W_IN_PIECE).T[None], 'w_glu_b': piece('w_glu').reshape(inp['w_glu_b'].shape),
             'w_out_ab': piece('w_out')[None], 'pool_w': piece('pool_w').reshape(inp['pool_w'].shape),
             'xa_wq': both('wq', ident), 'xa_wkv': both('wkv_t', transpose), 'xa_wo': both('wo', ident),
             'ffn_w_up': both('up_t', transpose), 'ffn_w_down': both('down', ident)}
    for n, off, size in zip(rep_names + small_names, misc_offs, misc_sizes):
        flat = misc_sum[off:off + _round_up(size, PACK_COLS) // PACK_COLS].reshape(-1)[:size]
        if n in REPLICATED:
            grads[n] = flat.reshape(inp[n].shape)
        else:
            shape, axis = SMALL_SHARDED[n]
            grads[n] = lax.dynamic_index_in_dim(_split_shards(flat.reshape(shape), axis), dev, 0, keepdims=False
                                                ).reshape(inp[n].shape)
    tiny_names = [n for n in WEIGHT_NAMES if n not in BIG_SHARDED]
    upd = {}
    for n in big_names:
        upd[n] = _adamw(inp[n], grads[n], inp['m_' + n], inp['v_' + n], "adamw_" + n)
    tiny_sizes = [int(np.prod(inp[n].shape)) for n in tiny_names]
    tiny_offs = _row_offsets(tiny_sizes)
    packs = [_pack_rows([src(n).astype(F32).reshape(-1) for n in tiny_names], 8)
             for src in (lambda n: inp[n], lambda n: grads[n], lambda n: inp['m_' + n], lambda n: inp['v_' + n])]
    tiny_out = _adamw(*packs, "adamw_small")
    for n, off, size in zip(tiny_names, tiny_offs, tiny_sizes):
        r = _round_up(size, PACK_COLS) // PACK_COLS
        upd[n] = tuple(o[off:off + r].reshape(-1)[:size].reshape(inp[n].shape) for o in tiny_out)

    loss = lax.psum(loss_part[0, 0], ("x", "y", "c"))
    outs = [loss, grad_x[None]]
    outs += [grads[n] for n in WEIGHT_NAMES]
    for i in range(3):
        outs += [upd[n][i] for n in WEIGHT_NAMES]
    return tuple(outs)


def _step(inp):
    loss_part, grad_x, big_reduced, rep_grads, small_grads = _local_step(inp)
    misc_sum, misc_sizes = _reduce_small_gradients(rep_grads, small_grads)
    return _update(inp, loss_part, grad_x, big_reduced, misc_sum, misc_sizes)


INPUT_NAMES = (['x', 'mem'] + WEIGHT_NAMES + ['loss_target'] + ['m_' + n for n in WEIGHT_NAMES]
               + ['v_' + n for n in WEIGHT_NAMES])


def kernel(x, mem, norm_mix_g, norm_xa_g, norm_ffn_g, norm_mem_g, norm_final_g, w_in_ab, conv_qkv_a, a_log_a, dt_bias_a, onorm_g_a, ssm_lambda_re, ssm_lambda_im, ssm_b_re, ssm_b_im, ssm_c_re, ssm_c_im, ssm_d, ssm_log_dt, w_glu_b, b_glu_b, w_out_ab, pool_w, pool_scale, xa_wq, xa_wkv, xa_wo, ffn_w_up, ffn_conv, ffn_w_down, loss_target, m_norm_mix_g, m_norm_xa_g, m_norm_ffn_g, m_norm_mem_g, m_norm_final_g, m_w_in_ab, m_conv_qkv_a, m_a_log_a, m_dt_bias_a, m_onorm_g_a, m_ssm_lambda_re, m_ssm_lambda_im, m_ssm_b_re, m_ssm_b_im, m_ssm_c_re, m_ssm_c_im, m_ssm_d, m_ssm_log_dt, m_w_glu_b, m_b_glu_b, m_w_out_ab, m_pool_w, m_pool_scale, m_xa_wq, m_xa_wkv, m_xa_wo, m_ffn_w_up, m_ffn_conv, m_ffn_w_down, v_norm_mix_g, v_norm_xa_g, v_norm_ffn_g, v_norm_mem_g, v_norm_final_g, v_w_in_ab, v_conv_qkv_a, v_a_log_a, v_dt_bias_a, v_onorm_g_a, v_ssm_lambda_re, v_ssm_lambda_im, v_ssm_b_re, v_ssm_b_im, v_ssm_c_re, v_ssm_c_im, v_ssm_d, v_ssm_log_dt, v_w_glu_b, v_b_glu_b, v_w_out_ab, v_pool_w, v_pool_scale, v_xa_wq, v_xa_wkv, v_xa_wo, v_ffn_w_up, v_ffn_conv, v_ffn_w_down):
    args = (x, mem, norm_mix_g, norm_xa_g, norm_ffn_g, norm_mem_g, norm_final_g, w_in_ab, conv_qkv_a, a_log_a, dt_bias_a, onorm_g_a, ssm_lambda_re, ssm_lambda_im, ssm_b_re, ssm_b_im, ssm_c_re, ssm_c_im, ssm_d, ssm_log_dt, w_glu_b, b_glu_b, w_out_ab, pool_w, pool_scale, xa_wq, xa_wkv, xa_wo, ffn_w_up, ffn_conv, ffn_w_down, loss_target, m_norm_mix_g, m_norm_xa_g, m_norm_ffn_g, m_norm_mem_g, m_norm_final_g, m_w_in_ab, m_conv_qkv_a, m_a_log_a, m_dt_bias_a, m_onorm_g_a, m_ssm_lambda_re, m_ssm_lambda_im, m_ssm_b_re, m_ssm_b_im, m_ssm_c_re, m_ssm_c_im, m_ssm_d, m_ssm_log_dt, m_w_glu_b, m_b_glu_b, m_w_out_ab, m_pool_w, m_pool_scale, m_xa_wq, m_xa_wkv, m_xa_wo, m_ffn_w_up, m_ffn_conv, m_ffn_w_down, v_norm_mix_g, v_norm_xa_g, v_norm_ffn_g, v_norm_mem_g, v_norm_final_g, v_w_in_ab, v_conv_qkv_a, v_a_log_a, v_dt_bias_a, v_onorm_g_a, v_ssm_lambda_re, v_ssm_lambda_im, v_ssm_b_re, v_ssm_b_im, v_ssm_c_re, v_ssm_c_im, v_ssm_d, v_ssm_log_dt, v_w_glu_b, v_b_glu_b, v_w_out_ab, v_pool_w, v_pool_scale, v_xa_wq, v_xa_wkv, v_xa_wo, v_ffn_w_up, v_ffn_conv, v_ffn_w_down)
    return _step(dict(zip(INPUT_NAMES, args)))
```

```python
import functools
import math

import numpy as np
import jax
import jax.numpy as jnp
from jax import lax
from jax.experimental import pallas as pl
from jax.experimental.pallas import tpu as pltpu

F32, BF16 = jnp.float32, jnp.bfloat16
HIGH, HIGHEST = lax.Precision.HIGH, lax.Precision.HIGHEST
MESH = pl.DeviceIdType.MESH

N_DEV = 8
SEQ, D_MODEL, MEM_LEN = 2048, 1024, 256
WIDTH_A, N_HEADS_A, HEAD_A, CONV_A = 512, 4, 128, 4
GDR_CHUNK = 128
SSM_WIDTH, SSM_GROUP, N_GROUPS, SSM_STATE = 512, 16, 32, 64
SSM_CH = N_GROUPS * SSM_STATE
SCAN_CB = 512
POOL_WINDOWS = (2, 4, 8, 16)
POOL_GROUP = 256
N_HEADS_X, HEAD_X = 4, 256
D_FF, CONV_FFN = 2816, 3
RMS_EPS = 1e-6
ADAM_LR, ADAM_B1, ADAM_B2, ADAM_EPS, ADAM_WD, ADAM_STEP = 0.001, 0.9, 0.999, 1e-08, 0.01, 10
LANE = 128
PACK_COLS = 1024
VMEM_LIMIT_BYTES = 56 * 1024 * 1024


def _params(sem=None):
    return pltpu.CompilerParams(dimension_semantics=sem, vmem_limit_bytes=VMEM_LIMIT_BYTES)


class Comm:
    def __init__(self, inputs, out_shapes, sems, start, end, mid=None):
        self.inputs, self.out_shapes, self.sems = list(inputs), list(out_shapes), list(sems)
        self.start, self.mid, self.end = start, mid, end


def _merge_comms(comms):
    comms = [c for c in comms if c is not None]
    if not comms:
        return None, []
    bounds, ni, no, ns = [], 0, 0, 0
    for c in comms:
        bounds.append((ni, no, ns))
        ni, no, ns = ni + len(c.inputs), no + len(c.out_shapes), ns + len(c.sems)

    def phase(which):
        def run(ins, outs, sems):
            for c, (i0, o0, s0) in zip(comms, bounds):
                fn = getattr(c, which)
                if fn is not None:
                    fn(ins[i0:i0 + len(c.inputs)], outs[o0:o0 + len(c.out_shapes)], sems[s0:s0 + len(c.sems)])
        return run

    merged = Comm([a for c in comms for a in c.inputs], [s for c in comms for s in c.out_shapes],
                  [s for c in comms for s in c.sems], phase("start"), phase("end"), phase("mid"))
    return merged, [(o0, o0 + len(c.out_shapes)) for c, (_, o0, _) in zip(comms, bounds)]


def _call(body, *, name, grid, in_specs, out_specs, out_shape, args, scratch_shapes=(), sem=None, comm=None):
    single = not isinstance(out_shape, (list, tuple))
    out_specs_l = [out_specs] if single else list(out_specs)
    out_shape_l = [out_shape] if single else list(out_shape)
    scratch_shapes = list(scratch_shapes)
    merged, spans = _merge_comms(comm if isinstance(comm, (list, tuple)) else [comm])
    if merged is None:
        outs = pl.pallas_call(body, name=name, grid=grid, in_specs=list(in_specs), out_specs=out_specs_l,
                              out_shape=out_shape_l, scratch_shapes=scratch_shapes, compiler_params=_params(sem))(*args)
        outs = outs[0] if single else outs
        return outs if comm is None else (outs, [])
    n_in, n_out, n_scr = len(in_specs), len(out_specs_l), len(scratch_shapes)
    ci, co = len(merged.inputs), len(merged.out_shapes)
    total = int(np.prod(grid))

    def wrapped(*refs):
        ins, cins = refs[:n_in], refs[n_in:n_in + ci]
        outs, couts = refs[n_in + ci:n_in + ci + n_out], refs[n_in + ci + n_out:n_in + ci + n_out + co]
        scr, csems = refs[n_in + ci + n_out + co:n_in + ci + n_out + co + n_scr], refs[n_in + ci + n_out + co + n_scr:]
        lin = pl.program_id(0)
        for d in range(1, len(grid)):
            lin = lin * grid[d] + pl.program_id(d)
        pl.when(lin == 0)(lambda: merged.start(cins, couts, csems))
        body(*ins, *outs, *scr)
        def finish():
            merged.mid(cins, couts, csems)
            merged.end(cins, couts, csems)

        pl.when(lin == total - 1)(finish)

    any_spec = pl.BlockSpec(memory_space=pl.ANY)
    res = pl.pallas_call(
        wrapped, name=name, grid=grid, in_specs=list(in_specs) + [any_spec] * ci,
        out_specs=out_specs_l + [any_spec] * co, out_shape=out_shape_l + merged.out_shapes,
        scratch_shapes=scratch_shapes + merged.sems,
        compiler_params=_params(("arbitrary",) * len(grid)))(*args, *merged.inputs)
    outs, couts = res[:n_out], res[n_out:]
    return (outs[0] if single else list(outs)), [list(couts[a:b]) for a, b in spans]


def _comm_only(comm, name):
    def body():
        pass

    _, couts = _call(body, name=name, grid=(1,), in_specs=[], out_specs=[], out_shape=[], args=[], comm=comm)
    return couts[0]


def _tile(dim, pref):
    best = None
    for t in range(LANE, min(dim, pref) + 1, LANE):
        if dim % t == 0:
            best = t
    return best if best is not None else dim


MM_VMEM_BUDGET = 40 * 1024 * 1024


def _mm_tiles(m, n, k, a_bytes, b_bytes, o_bytes, r_bytes):
    for tk in (k, _tile(k, 2048), _tile(k, 1024), _tile(k, 512)):
        for tm, tn in ((1024, 1536), (1024, 1024), (1024, 512), (512, 512), (256, 512), (256, 256)):
            tm, tn = _tile(m, tm), _tile(n, tn)
            acc = 0 if tk == k else tm * tn * 4
            need = 2 * (tm * tk * a_bytes + tk * tn * b_bytes + tm * tn * (o_bytes + r_bytes)) + acc
            if need <= MM_VMEM_BUDGET:
                return tm, tn, tk
    raise ValueError("no matmul tiling fits VMEM")


def _mm(a, b, mode, name, out_dtype=F32, res=None, comm=None):
    if mode == "nn":
        (m, k), n = a.shape, b.shape[1]
    elif mode == "nt":
        (m, k), n = a.shape, b.shape[0]
    else:
        (k, m), n = a.shape, b.shape[1]
    tm, tn, tk = _mm_tiles(m, n, k, a.dtype.itemsize, b.dtype.itemsize, jnp.dtype(out_dtype).itemsize,
                           0 if res is None else res.dtype.itemsize)
    nk = k // tk
    dims = {"nn": ((1,), (0,)), "nt": ((1,), (1,)), "tn": ((0,), (0,))}[mode]

    def body(*refs):
        if res is None:
            a_ref, b_ref, o_ref = refs[:3]
            r_ref = None
        else:
            a_ref, b_ref, r_ref, o_ref = refs[:4]
        part = lax.dot_general(a_ref[...].astype(BF16), b_ref[...].astype(BF16), (dims, ((), ())),
                               preferred_element_type=F32)

        def finish(out):
            if r_ref is not None:
                out = out + r_ref[...].astype(F32)
            o_ref[...] = out.astype(out_dtype)

        if nk == 1:
            finish(part)
            return
        acc = refs[-1]
        kk = pl.program_id(2)

        @pl.when(kk == 0)
        def _():
            acc[...] = part

        @pl.when(kk > 0)
        def _():
            acc[...] += part

        @pl.when(kk == nk - 1)
        def _():
            finish(acc[...])

    a_spec = (pl.BlockSpec((tk, tm), lambda i, j, q: (q, i)) if mode == "tn"
              else pl.BlockSpec((tm, tk), lambda i, j, q: (i, q)))
    b_spec = (pl.BlockSpec((tn, tk), lambda i, j, q: (j, q)) if mode == "nt"
              else pl.BlockSpec((tk, tn), lambda i, j, q: (q, j)))
    o_spec = pl.BlockSpec((tm, tn), lambda i, j, q: (i, j))
    in_specs, args = [a_spec, b_spec], [a, b]
    if res is not None:
        in_specs.append(o_spec)
        args.append(res)
    return _call(body, name=name, grid=(m // tm, n // tn, nk), in_specs=in_specs, out_specs=o_spec,
                 out_shape=jax.ShapeDtypeStruct((m, n), out_dtype),
                 scratch_shapes=[] if nk == 1 else [pltpu.VMEM((tm, tn), F32)],
                 sem=("parallel", "parallel", "arbitrary"), args=args, comm=comm)


def _rms_fwd(x, g, out_dtype, name, tr=256):
    rows, d = x.shape

    def body(x_ref, g_ref, o_ref):
        xv = x_ref[...]
        r = lax.rsqrt(jnp.mean(xv * xv, axis=-1, keepdims=True) + RMS_EPS)
        o_ref[...] = (xv * r * g_ref[...]).astype(out_dtype)

    return pl.pallas_call(
        body, name=name, grid=(rows // tr,),
        in_specs=[pl.BlockSpec((tr, d), lambda i: (i, 0)), pl.BlockSpec((1, d), lambda i: (0, 0))],
        out_specs=pl.BlockSpec((tr, d), lambda i: (i, 0)), out_shape=jax.ShapeDtypeStruct((rows, d), out_dtype),
        compiler_params=_params(("parallel",)))(x, g.reshape(1, d))


def _rms_bwd(x, g, dy, dres, name, tr=256, comm=None):
    rows, d = x.shape

    def body(*refs):
        if dres is None:
            x_ref, g_ref, dy_ref, dx_ref, dg_ref = refs
            r_ref = None
        else:
            x_ref, g_ref, dy_ref, r_ref, dx_ref, dg_ref = refs

        @pl.when(pl.program_id(0) == 0)
        def _():
            dg_ref[...] = jnp.zeros_like(dg_ref)

        xv, dyv = x_ref[...], dy_ref[...].astype(F32)
        r = lax.rsqrt(jnp.mean(xv * xv, axis=-1, keepdims=True) + RMS_EPS)
        xh = xv * r
        dyg = dyv * g_ref[...]
        dx = r * (dyg - xh * jnp.mean(dyg * xh, axis=-1, keepdims=True))
        if r_ref is not None:
            dx = dx + r_ref[...]
        dx_ref[...] = dx
        dg_ref[...] += jnp.sum(dyv * xh, axis=0, keepdims=True)

    blk = pl.BlockSpec((tr, d), lambda i: (i, 0))
    vec = pl.BlockSpec((1, d), lambda i: (0, 0))
    in_specs, args = [blk, vec, blk], [x, g.reshape(1, d), dy]
    if dres is not None:
        in_specs.append(blk)
        args.append(dres)
    return _call(
        body, name=name, grid=(rows // tr,), in_specs=in_specs, out_specs=[blk, vec],
        out_shape=[jax.ShapeDtypeStruct((rows, d), F32), jax.ShapeDtypeStruct((1, d), F32)],
        sem=("arbitrary",), args=args, comm=comm)


def _loss_head(x, g, target, name, tr=256):
    rows, d = x.shape

    def body(x_ref, g_ref, t_ref, loss_ref, dx_ref, dg_ref):
        @pl.when(pl.program_id(0) == 0)
        def _():
            dg_ref[...] = jnp.zeros_like(dg_ref)
            loss_ref[...] = jnp.zeros_like(loss_ref)

        xv = x_ref[...]
        r = lax.rsqrt(jnp.mean(xv * xv, axis=-1, keepdims=True) + RMS_EPS)
        xh = xv * r
        err = xh * g_ref[...] - t_ref[...]
        loss_ref[...] += 0.5 * jnp.sum(jnp.mean(err * err, axis=-1, keepdims=True), keepdims=True)
        dyv = err * (1.0 / d)
        dyg = dyv * g_ref[...]
        dx_ref[...] = r * (dyg - xh * jnp.mean(dyg * xh, axis=-1, keepdims=True))
        dg_ref[...] += jnp.sum(dyv * xh, axis=0, keepdims=True)

    blk = pl.BlockSpec((tr, d), lambda i: (i, 0))
    vec = pl.BlockSpec((1, d), lambda i: (0, 0))
    return pl.pallas_call(
        body, name=name, grid=(rows // tr,), in_specs=[blk, vec, blk],
        out_specs=[pl.BlockSpec((1, 1), lambda i: (0, 0)), blk, vec],
        out_shape=[jax.ShapeDtypeStruct((1, 1), F32), jax.ShapeDtypeStruct((rows, d), F32),
                   jax.ShapeDtypeStruct((1, d), F32)],
        compiler_params=_params(("arbitrary",)))(x, g.reshape(1, d), target)


def _shift_down(x, s):
    rows = lax.broadcasted_iota(jnp.int32, x.shape, 0)
    return jnp.where(rows >= s, pltpu.roll(x, s, 0), 0.0)


def _shift_up(x, s):
    n = x.shape[0]
    rows = lax.broadcasted_iota(jnp.int32, x.shape, 0)
    return jnp.where(rows < n - s, pltpu.roll(x, n - s, 0), 0.0)


def _sigmoid(x):
    return 1.0 / (1.0 + jnp.exp(-x))


def _silu_and_grad(x):
    s = _sigmoid(x)
    return x * s, s * (1.0 + x * (1.0 - s))


_GELU_C0, _GELU_C1 = math.sqrt(2.0 / math.pi), 0.044715


def _gelu_and_grad(x):
    th = jnp.tanh(_GELU_C0 * (x + _GELU_C1 * x * x * x))
    y = 0.5 * x * (1.0 + th)
    dy = 0.5 * (1.0 + th) + 0.5 * x * (1.0 - th * th) * _GELU_C0 * (1.0 + 3.0 * _GELU_C1 * x * x)
    return y, dy


def _ffn_act_fwd(h, w, name, tc=256, comm=None):
    t = h.shape[0]
    nb = D_FF // tc

    def body(hg_ref, hv_ref, wg_ref, wv_ref, a_ref):
        def conv(x, wr):
            return wr[2:3, :] * x + wr[1:2, :] * _shift_down(x, 1) + wr[0:1, :] * _shift_down(x, 2)

        cg = conv(hg_ref[...], wg_ref[...])
        cv = conv(hv_ref[...], wv_ref[...])
        a_ref[...] = (cg * _sigmoid(cg) * cv).astype(BF16)

    return _call(
        body, name=name, grid=(nb,),
        in_specs=[pl.BlockSpec((t, tc), lambda j: (0, j)), pl.BlockSpec((t, tc), lambda j: (0, j + nb)),
                  pl.BlockSpec((CONV_FFN, tc), lambda j: (0, j)), pl.BlockSpec((CONV_FFN, tc), lambda j: (0, j + nb))],
        out_specs=pl.BlockSpec((t, tc), lambda j: (0, j)), out_shape=jax.ShapeDtypeStruct((t, D_FF), BF16),
        sem=("parallel",), args=(h, h, w, w), comm=comm)


def _ffn_act_bwd(h, w, da, name, tc=256, comm=None):
    t = h.shape[0]
    nb = D_FF // tc

    def body(hg_ref, hv_ref, wg_ref, wv_ref, da_ref, dhg_ref, dhv_ref, dwg_ref, dwv_ref):
        hg, hv, wg, wv = hg_ref[...], hv_ref[...], wg_ref[...], wv_ref[...]
        hg1, hg2, hv1, hv2 = _shift_down(hg, 1), _shift_down(hg, 2), _shift_down(hv, 1), _shift_down(hv, 2)
        cg = wg[2:3, :] * hg + wg[1:2, :] * hg1 + wg[0:1, :] * hg2
        cv = wv[2:3, :] * hv + wv[1:2, :] * hv1 + wv[0:1, :] * hv2
        sg, dsg = _silu_and_grad(cg)
        dav = da_ref[...].astype(F32)
        dcv = dav * sg
        dcg = dav * cv * dsg

        def conv_t(dc, wr):
            return wr[2:3, :] * dc + wr[1:2, :] * _shift_up(dc, 1) + wr[0:1, :] * _shift_up(dc, 2)

        dhg_ref[...] = conv_t(dcg, wg).astype(BF16)
        dhv_ref[...] = conv_t(dcv, wv).astype(BF16)
        dwg_ref[0:1, :] = jnp.sum(dcg * hg2, axis=0, keepdims=True)
        dwg_ref[1:2, :] = jnp.sum(dcg * hg1, axis=0, keepdims=True)
        dwg_ref[2:3, :] = jnp.sum(dcg * hg, axis=0, keepdims=True)
        dwv_ref[0:1, :] = jnp.sum(dcv * hv2, axis=0, keepdims=True)
        dwv_ref[1:2, :] = jnp.sum(dcv * hv1, axis=0, keepdims=True)
        dwv_ref[2:3, :] = jnp.sum(dcv * hv, axis=0, keepdims=True)

    big = lambda off: pl.BlockSpec((t, tc), lambda j: (0, j + off))
    small = lambda off: pl.BlockSpec((CONV_FFN, tc), lambda j: (0, j + off))
    res = _call(
        body, name=name, grid=(nb,),
        in_specs=[big(0), big(nb), small(0), small(nb), big(0)],
        out_specs=[big(0), big(0), small(0), small(0)],
        out_shape=[jax.ShapeDtypeStruct((t, D_FF), BF16), jax.ShapeDtypeStruct((t, D_FF), BF16),
                   jax.ShapeDtypeStruct((CONV_FFN, D_FF), F32), jax.ShapeDtypeStruct((CONV_FFN, D_FF), F32)],
        sem=("parallel",), args=(h, h, w, w, da), comm=comm)
    (dhg, dhv, dwg, dwv), couts = res if comm is not None else (res, None)
    out = (jnp.concatenate([dhg, dhv], axis=1), jnp.concatenate([dwg, dwv], axis=1))
    return out if comm is None else (out, couts)


def _attn_probs(q, k):
    s = lax.dot_general(q.astype(BF16), k.astype(BF16), (((1,), (1,)), ((), ())),
                        preferred_element_type=F32) * (HEAD_X ** -0.5)
    s = s - jnp.max(s, axis=-1, keepdims=True)
    p = jnp.exp(s)
    return p / jnp.sum(p, axis=-1, keepdims=True)


def _attn_fwd(q, kv, name, tq=512, comm=None):
    t = q.shape[0]

    def body(q_ref, k_ref, v_ref, o_ref):
        p = _attn_probs(q_ref[...], k_ref[...])
        o_ref[...] = jnp.dot(p.astype(BF16), v_ref[...].astype(BF16), preferred_element_type=F32).astype(BF16)

    return _call(
        body, name=name, grid=(N_HEADS_X, t // tq),
        in_specs=[pl.BlockSpec((tq, HEAD_X), lambda h, i: (i, h)),
                  pl.BlockSpec((MEM_LEN, HEAD_X), lambda h, i: (0, h)),
                  pl.BlockSpec((MEM_LEN, HEAD_X), lambda h, i: (0, h + N_HEADS_X))],
        out_specs=pl.BlockSpec((tq, HEAD_X), lambda h, i: (i, h)),
        out_shape=jax.ShapeDtypeStruct((t, N_HEADS_X * HEAD_X), BF16),
        sem=("parallel", "parallel"), args=(q, kv, kv), comm=comm)


def _attn_bwd(q, kv, do, name, tq=512):
    t = q.shape[0]

    def body(q_ref, k_ref, v_ref, do_ref, dq_ref, dk_ref, dv_ref):
        @pl.when(pl.program_id(1) == 0)
        def _():
            dk_ref[...] = jnp.zeros_like(dk_ref)
            dv_ref[...] = jnp.zeros_like(dv_ref)

        qb, kb, vb, dob = (r[...].astype(BF16) for r in (q_ref, k_ref, v_ref, do_ref))
        p = _attn_probs(qb, kb)
        dp = lax.dot_general(dob, vb, (((1,), (1,)), ((), ())), preferred_element_type=F32)
        ds = p * (dp - jnp.sum(dp * p, axis=-1, keepdims=True)) * (HEAD_X ** -0.5)
        dsb = ds.astype(BF16)
        dq_ref[...] = jnp.dot(dsb, kb, preferred_element_type=F32).astype(BF16)
        dk_ref[...] += lax.dot_general(dsb, qb, (((0,), (0,)), ((), ())), preferred_element_type=F32)
        dv_ref[...] += lax.dot_general(p.astype(BF16), dob, (((0,), (0,)), ((), ())), preferred_element_type=F32)

    qs = pl.BlockSpec((tq, HEAD_X), lambda h, i: (i, h))
    ms = pl.BlockSpec((MEM_LEN, HEAD_X), lambda h, i: (0, h))
    return pl.pallas_call(
        body, name=name, grid=(N_HEADS_X, t // tq),
        in_specs=[qs, ms, pl.BlockSpec((MEM_LEN, HEAD_X), lambda h, i: (0, h + N_HEADS_X)), qs],
        out_specs=[qs, ms, ms],
        out_shape=[jax.ShapeDtypeStruct((t, D_MODEL), BF16), jax.ShapeDtypeStruct((MEM_LEN, D_MODEL), F32),
                   jax.ShapeDtypeStruct((MEM_LEN, D_MODEL), F32)],
        compiler_params=_params(("parallel", "arbitrary")))(q, kv, kv, do)


def _pool_counts(t, win):
    pos = lax.broadcasted_iota(jnp.int32, (t, 1), 0).astype(F32) + 1.0
    return 1.0 / jnp.minimum(pos, float(win))


def _pool_delta(xv, win):
    s, step = xv, 1
    while step < win:
        s = s + _shift_down(s, step)
        step *= 2
    return s * _pool_counts(xv.shape[0], win) - xv


def _pool_delta_t(dv, win):
    s, step = dv * _pool_counts(dv.shape[0], win), 1
    while step < win:
        s = s + _shift_up(s, step)
        step *= 2
    return s - dv


def _pool_fwd(xn, w, scale, res, name):
    t = xn.shape[0]

    def make_branch(win, xn_ref, w_ref, s_ref, r_ref, o_ref):
        def branch():
            dl = _pool_delta(xn_ref[...], win)
            y = jnp.dot(dl.astype(BF16), w_ref[0], preferred_element_type=F32)
            o_ref[...] = r_ref[...] + y * s_ref[...]
        return branch

    def body(xn_ref, w_ref, s_ref, r_ref, o_ref):
        for gi, win in enumerate(POOL_WINDOWS):
            pl.when(pl.program_id(0) == gi)(make_branch(win, xn_ref, w_ref, s_ref, r_ref, o_ref))

    blk = pl.BlockSpec((t, POOL_GROUP), lambda g: (0, g))
    return pl.pallas_call(
        body, name=name, grid=(len(POOL_WINDOWS),),
        in_specs=[blk, pl.BlockSpec((1, POOL_GROUP, POOL_GROUP), lambda g: (g, 0, 0)),
                  pl.BlockSpec((1, POOL_GROUP), lambda g: (0, g)), blk],
        out_specs=blk, out_shape=jax.ShapeDtypeStruct((t, D_MODEL), F32),
        compiler_params=_params(("parallel",)))(xn, w, scale, res)


def _pool_bwd(xn, w, scale, dmix, name):
    t = xn.shape[0]

    def make_branch(win, xn_ref, w_ref, s_ref, d_ref, dxn_ref, dw_ref, ds_ref):
        def branch():
            dl = _pool_delta(xn_ref[...], win).astype(BF16)
            wv = w_ref[0]
            dm = d_ref[...]
            y = jnp.dot(dl, wv, preferred_element_type=F32)
            ds_ref[...] = jnp.sum(dm * y, axis=0, keepdims=True)
            dy = (dm * s_ref[...]).astype(BF16)
            dw_ref[0] = lax.dot_general(dl, dy, (((0,), (0,)), ((), ())), preferred_element_type=F32)
            ddl = lax.dot_general(dy, wv, (((1,), (1,)), ((), ())), preferred_element_type=F32)
            dxn_ref[...] = _pool_delta_t(ddl, win)
        return branch

    def body(*refs):
        for gi, win in enumerate(POOL_WINDOWS):
            pl.when(pl.program_id(0) == gi)(make_branch(win, *refs))

    blk = pl.BlockSpec((t, POOL_GROUP), lambda g: (0, g))
    wspec = pl.BlockSpec((1, POOL_GROUP, POOL_GROUP), lambda g: (g, 0, 0))
    vec = pl.BlockSpec((1, POOL_GROUP), lambda g: (0, g))
    return pl.pallas_call(
        body, name=name, grid=(len(POOL_WINDOWS),), in_specs=[blk, wspec, vec, blk], out_specs=[blk, wspec, vec],
        out_shape=[jax.ShapeDtypeStruct((t, D_MODEL), F32),
                   jax.ShapeDtypeStruct((len(POOL_WINDOWS), POOL_GROUP, POOL_GROUP), F32),
                   jax.ShapeDtypeStruct((1, D_MODEL), F32)],
        compiler_params=_params(("parallel",)))(xn, w, scale, dmix)


def _qkv_conv(h, wr):
    return (wr[3:4, :] * h + wr[2:3, :] * _shift_down(h, 1) + wr[1:2, :] * _shift_down(h, 2)
            + wr[0:1, :] * _shift_down(h, 3))


def _qkv_pre_fwd(h, w, col0, ncols, normalize, scale, name):
    t = h.shape[0]

    def body(h_ref, w_ref, o_ref):
        c = _qkv_conv(h_ref[...], w_ref[...])
        s = c * _sigmoid(c)
        if normalize:
            s = s * lax.rsqrt(jnp.sum(s * s, axis=-1, keepdims=True) + 1e-6) * scale
        o_ref[...] = s

    return pl.pallas_call(
        body, name=name, grid=(ncols,),
        in_specs=[pl.BlockSpec((t, HEAD_A), lambda j: (0, j + col0)), pl.BlockSpec((CONV_A, HEAD_A), lambda j: (0, j + col0))],
        out_specs=pl.BlockSpec((t, HEAD_A), lambda j: (0, j)), out_shape=jax.ShapeDtypeStruct((t, ncols * HEAD_A), F32),
        compiler_params=_params(("parallel",)))(h, w)


def _qkv_pre_bwd(h, w, dy, col0, ncols, normalize, scale, name):
    t = h.shape[0]

    def body(h_ref, w_ref, dy_ref, dh_ref, dw_ref):
        hv, wr, dyv = h_ref[...], w_ref[...], dy_ref[...]
        h1, h2, h3 = _shift_down(hv, 1), _shift_down(hv, 2), _shift_down(hv, 3)
        c = wr[3:4, :] * hv + wr[2:3, :] * h1 + wr[1:2, :] * h2 + wr[0:1, :] * h3
        s, dsilu = _silu_and_grad(c)
        if normalize:
            r = lax.rsqrt(jnp.sum(s * s, axis=-1, keepdims=True) + 1e-6)
            y = s * r
            dyv = dyv * scale
            ds = r * (dyv - y * jnp.sum(dyv * y, axis=-1, keepdims=True))
        else:
            ds = dyv
        dc = ds * dsilu
        dh = (wr[3:4, :] * dc + wr[2:3, :] * _shift_up(dc, 1) + wr[1:2, :] * _shift_up(dc, 2)
              + wr[0:1, :] * _shift_up(dc, 3))
        dh_ref[...] = dh.astype(BF16)
        dw_ref[0:1, :] = jnp.sum(dc * h3, axis=0, keepdims=True)
        dw_ref[1:2, :] = jnp.sum(dc * h2, axis=0, keepdims=True)
        dw_ref[2:3, :] = jnp.sum(dc * h1, axis=0, keepdims=True)
        dw_ref[3:4, :] = jnp.sum(dc * hv, axis=0, keepdims=True)

    return pl.pallas_call(
        body, name=name, grid=(ncols,),
        in_specs=[pl.BlockSpec((t, HEAD_A), lambda j: (0, j + col0)), pl.BlockSpec((CONV_A, HEAD_A), lambda j: (0, j + col0)),
                  pl.BlockSpec((t, HEAD_A), lambda j: (0, j))],
        out_specs=[pl.BlockSpec((t, HEAD_A), lambda j: (0, j)), pl.BlockSpec((CONV_A, HEAD_A), lambda j: (0, j))],
        out_shape=[jax.ShapeDtypeStruct((t, ncols * HEAD_A), BF16), jax.ShapeDtypeStruct((CONV_A, ncols * HEAD_A), F32)],
        compiler_params=_params(("parallel",)))(h, w, dy)


def _softplus(x):
    return jnp.maximum(x, 0.0) + jnp.log1p(jnp.exp(-jnp.abs(x)))


def _gates_fwd(ba, arow, brow, name):
    t = ba.shape[0]

    def body(x_ref, a_ref, b_ref, o_ref):
        xv = x_ref[...]
        lane = lax.broadcasted_iota(jnp.int32, xv.shape, 1)
        beta = _sigmoid(xv)
        g = -jnp.exp(a_ref[...]) * _softplus(xv + b_ref[...])
        o_ref[...] = jnp.where(lane < N_HEADS_A, beta, jnp.where(lane < 2 * N_HEADS_A, g, 0.0))

    return pl.pallas_call(body, name=name, out_shape=jax.ShapeDtypeStruct((t, LANE), F32),
                          compiler_params=_params())(ba, arow, brow)


def _gates_bwd(ba, arow, brow, dgb, name):
    t = ba.shape[0]

    def body(x_ref, a_ref, b_ref, d_ref, dx_ref, da_ref, db_ref):
        xv = x_ref[...]
        dv = d_ref[0] + d_ref[1] + d_ref[2] + d_ref[3]
        lane = lax.broadcasted_iota(jnp.int32, xv.shape, 1)
        beta = _sigmoid(xv)
        ea = jnp.exp(a_ref[...])
        z = xv + b_ref[...]
        dgv = jnp.where((lane >= N_HEADS_A) & (lane < 2 * N_HEADS_A), dv, 0.0) * (-ea)
        dz = dgv * _sigmoid(z)
        dx = jnp.where(lane < N_HEADS_A, dv * beta * (1.0 - beta), dz)
        dx_ref[...] = dx.astype(BF16)
        db_ref[...] = jnp.sum(dz, axis=0, keepdims=True)
        da_ref[...] = jnp.sum(dgv * _softplus(z), axis=0, keepdims=True)

    return pl.pallas_call(
        body, name=name,
        out_shape=[jax.ShapeDtypeStruct((t, LANE), BF16), jax.ShapeDtypeStruct((1, LANE), F32),
                   jax.ShapeDtypeStruct((1, LANE), F32)],
        compiler_params=_params())(ba, arow, brow, dgb)


def _dot(a, b, prec=None):
    if prec is None:
        return jnp.dot(a.astype(BF16), b.astype(BF16), preferred_element_type=F32)
    return jnp.dot(a, b, precision=prec, preferred_element_type=F32)


def _dot_nt(a, b, prec=None):
    if prec is None:
        a, b = a.astype(BF16), b.astype(BF16)
    return lax.dot_general(a, b, (((1,), (1,)), ((), ())), precision=prec, preferred_element_type=F32)


def _dot_tn(a, b, prec=None):
    if prec is None:
        a, b = a.astype(BF16), b.astype(BF16)
    return lax.dot_general(a, b, (((0,), (0,)), ((), ())), precision=prec, preferred_element_type=F32)


def _gdr_chunk_terms(k, beta, g):
    c = GDR_CHUNK
    row = lax.broadcasted_iota(jnp.int32, (c, c), 0)
    col = lax.broadcasted_iota(jnp.int32, (c, c), 1)
    causal, strict = row >= col, row > col
    gcum = _dot(causal.astype(F32), jnp.broadcast_to(g, (c, c)), HIGHEST)
    diff = gcum - gcum.T
    decay = jnp.where(causal, jnp.exp(jnp.where(causal, diff, 0.0)), 0.0)
    kb = k * beta
    kk = _dot_nt(kb, k)
    return row, col, causal, strict, gcum, decay, kb, kk


def _unit_lower_inverse(a):
    c = a.shape[0]
    eye = (lax.broadcasted_iota(jnp.int32, (c, c), 0) == lax.broadcasted_iota(jnp.int32, (c, c), 1)).astype(F32)
    p = -a
    inv = eye + p
    step = 1
    while 2 * step < c:
        p = _dot(p, p, HIGH)
        inv = inv + _dot(inv, p, HIGH)
        step *= 2
    return inv


def _head_gates(gates, head):
    lane = lax.broadcasted_iota(jnp.int32, gates.shape, 1)
    beta = jnp.sum(jnp.where(lane == head, gates, 0.0), axis=1, keepdims=True)
    g = jnp.sum(jnp.where(lane == head + N_HEADS_A, gates, 0.0), axis=1, keepdims=True)
    return beta, g


def _gdr_fwd(q, k, v, gates, name, comm=None):
    t = q.shape[0]
    c = GDR_CHUNK
    n = t // c

    def body(q_ref, k_ref, v_ref, gb_ref, o_ref, tm_ref, s_ref, state):
        @pl.when(pl.program_id(1) == 0)
        def _():
            state[...] = jnp.zeros_like(state)

        qv, kv, vv = q_ref[...], k_ref[...], v_ref[...]
        beta, g = _head_gates(gb_ref[...], pl.program_id(0))
        row, col, causal, strict, gcum, decay, kb, kk = _gdr_chunk_terms(kv, beta, g)
        tm = _unit_lower_inverse(jnp.where(strict, kk * decay, 0.0))
        e = jnp.exp(gcum)
        u = _dot(tm, vv * beta, HIGH)
        w = _dot(tm, kb * e, HIGH)
        p = jnp.where(causal, _dot_nt(qv, kv) * decay, 0.0)
        s = state[...]
        s_ref[0, 0] = s
        tm_ref[0, 0] = tm
        vn = u - _dot(w, s)
        o_ref[...] = _dot(qv * e, s) + _dot(p, vn)
        glast = gcum[c - 1:c, :]
        state[...] = s * jnp.exp(glast) + _dot_tn(kv * jnp.exp(glast - gcum), vn)

    blk = pl.BlockSpec((c, HEAD_A), lambda h, i: (i, h))
    mat = pl.BlockSpec((1, 1, c, c), lambda h, i: (h, i, 0, 0))
    return _call(
        body, name=name, grid=(N_HEADS_A, n),
        in_specs=[blk, blk, blk, pl.BlockSpec((c, LANE), lambda h, i: (i, 0))],
        out_specs=[blk, mat, mat],
        out_shape=[jax.ShapeDtypeStruct((t, WIDTH_A), F32), jax.ShapeDtypeStruct((N_HEADS_A, n, c, c), F32),
                   jax.ShapeDtypeStruct((N_HEADS_A, n, HEAD_A, HEAD_A), F32)],
        scratch_shapes=[pltpu.VMEM((HEAD_A, HEAD_A), F32)], sem=("parallel", "arbitrary"),
        args=(q, k, v, gates), comm=comm)


def _gdr_bwd(q, k, v, gates, tm_all, s_all, do, name, comm=None):
    t = q.shape[0]
    c = GDR_CHUNK
    n = t // c

    def body(q_ref, k_ref, v_ref, gb_ref, tm_ref, s_ref, do_ref, dq_ref, dk_ref, dv_ref, dgb_ref, dstate):
        @pl.when(pl.program_id(1) == 0)
        def _():
            dstate[...] = jnp.zeros_like(dstate)

        qv, kv, vv, dov = q_ref[...], k_ref[...], v_ref[...], do_ref[...]
        head = pl.program_id(0)
        beta, g = _head_gates(gb_ref[...], head)
        tm, s, dsp = tm_ref[0, 0], s_ref[0, 0], dstate[...]
        row, col, causal, strict, gcum, decay, kb, kk = _gdr_chunk_terms(kv, beta, g)
        e = jnp.exp(gcum)
        vb, kbe = vv * beta, kb * e
        u = _dot(tm, vb, HIGH)
        w = _dot(tm, kbe, HIGH)
        qk = _dot_nt(qv, kv)
        p = jnp.where(causal, qk * decay, 0.0)
        vn = u - _dot(w, s)
        glast = gcum[c - 1:c, :]
        el = jnp.exp(glast)
        f = jnp.exp(glast - gcum)
        kd = kv * f
        qe = qv * e

        dvn = _dot_tn(p, dov) + _dot(kd, dsp)
        dglast = el[:, 0:1] * jnp.sum(s * dsp, keepdims=True)
        dkd = _dot_nt(vn, dsp)
        dk = dkd * f
        df = jnp.sum(dkd * kv, axis=1, keepdims=True) * f[:, 0:1]
        dglast = dglast + jnp.sum(df, keepdims=True)
        dgc = -df
        dp = jnp.where(causal, _dot_nt(dov, vn), 0.0)
        dqe = _dot_nt(dov, s)
        dq = dqe * e
        de = jnp.sum(dqe * qv, axis=1, keepdims=True)
        dstate[...] = dsp * el + _dot_tn(qe, dov) - _dot_tn(w, dvn)
        dw = -_dot_nt(dvn, s)
        dvb = _dot_tn(tm, dvn, HIGH)
        dkbe = _dot_tn(tm, dw, HIGH)
        da = -jnp.where(strict, _dot_nt(dvb, u) + _dot_nt(dkbe, w), 0.0)
        dkk = da * decay
        dqk = dp * decay
        dd = da * kk + dp * qk
        dq = dq + _dot(dqk, kv)
        dk = dk + _dot_tn(dqk, qv)
        dkb = _dot(dkk, kv) + dkbe * e
        dk = dk + _dot_tn(dkk, kb)
        de = de + jnp.sum(dkbe * kb, axis=1, keepdims=True)
        dk = dk + dkb * beta
        dbeta = jnp.sum(dkb * kv, axis=1, keepdims=True) + jnp.sum(dvb * vv, axis=1, keepdims=True)
        m = dd * decay
        dgc = dgc + jnp.sum(m, axis=1, keepdims=True) - jnp.sum(m.T, axis=1, keepdims=True)
        dgc = dgc + de * e[:, 0:1]
        dgc = dgc + jnp.where(row[:, 0:1] == c - 1, dglast, 0.0)
        dg = _dot((row <= col).astype(F32), jnp.broadcast_to(dgc, (c, c)), HIGHEST)
        dq_ref[...] = dq
        dk_ref[...] = dk
        dv_ref[...] = dvb * beta
        lane = lax.broadcasted_iota(jnp.int32, (c, LANE), 1)
        dgb_ref[0] = jnp.where(lane == head, dbeta, jnp.where(lane == head + N_HEADS_A, dg, 0.0))

    blk = pl.BlockSpec((c, HEAD_A), lambda h, i: (n - 1 - i, h))
    mat = pl.BlockSpec((1, 1, c, c), lambda h, i: (h, n - 1 - i, 0, 0))
    return _call(
        body, name=name, grid=(N_HEADS_A, n),
        in_specs=[blk, blk, blk, pl.BlockSpec((c, LANE), lambda h, i: (n - 1 - i, 0)), mat, mat, blk],
        out_specs=[blk, blk, blk, pl.BlockSpec((1, c, LANE), lambda h, i: (h, n - 1 - i, 0))],
        out_shape=[jax.ShapeDtypeStruct((t, WIDTH_A), F32)] * 3 + [jax.ShapeDtypeStruct((N_HEADS_A, t, LANE), F32)],
        scratch_shapes=[pltpu.VMEM((HEAD_A, HEAD_A), F32)], sem=("parallel", "arbitrary"),
        args=(q, k, v, gates, tm_all, s_all, do), comm=comm)


def _onorm_fwd(o, gate, g, name):
    t = o.shape[0]

    def body(o_ref, gate_ref, g_ref, y_ref):
        ov, gv = o_ref[...], gate_ref[...]
        r = lax.rsqrt(jnp.mean(ov * ov, axis=-1, keepdims=True) + RMS_EPS)
        y_ref[...] = (ov * r * g_ref[...] * gv * _sigmoid(gv)).astype(BF16)

    blk = pl.BlockSpec((t, HEAD_A), lambda j: (0, j))
    return pl.pallas_call(
        body, name=name, grid=(N_HEADS_A,), in_specs=[blk, blk, pl.BlockSpec((1, HEAD_A), lambda j: (0, 0))],
        out_specs=blk, out_shape=jax.ShapeDtypeStruct((t, WIDTH_A), BF16),
        compiler_params=_params(("parallel",)))(o, gate, g)


def _onorm_bwd(o, gate, g, dy, name):
    t = o.shape[0]

    def body(o_ref, gate_ref, g_ref, dy_ref, do_ref, dgate_ref, dg_ref):
        @pl.when(pl.program_id(0) == 0)
        def _():
            dg_ref[...] = jnp.zeros_like(dg_ref)

        ov, gv, dyv = o_ref[...], gate_ref[...], dy_ref[...].astype(F32)
        r = lax.rsqrt(jnp.mean(ov * ov, axis=-1, keepdims=True) + RMS_EPS)
        oh = ov * r
        sg, dsg = _silu_and_grad(gv)
        dgate_ref[...] = (dyv * oh * g_ref[...] * dsg).astype(BF16)
        dn = dyv * sg
        dg_ref[...] += jnp.sum(dn * oh, axis=0, keepdims=True)
        dng = dn * g_ref[...]
        do_ref[...] = r * (dng - oh * jnp.mean(dng * oh, axis=-1, keepdims=True))

    blk = pl.BlockSpec((t, HEAD_A), lambda j: (0, j))
    vec = pl.BlockSpec((1, HEAD_A), lambda j: (0, 0))
    return pl.pallas_call(
        body, name=name, grid=(N_HEADS_A,), in_specs=[blk, blk, vec, blk], out_specs=[blk, blk, vec],
        out_shape=[jax.ShapeDtypeStruct((t, WIDTH_A), F32), jax.ShapeDtypeStruct((t, WIDTH_A), BF16),
                   jax.ShapeDtypeStruct((1, HEAD_A), F32)],
        compiler_params=_params(("arbitrary",)))(o, gate, g, dy)


def _cmul(ar, ai, br, bi):
    return ar * br - ai * bi, ar * bi + ai * br


def _scan_tables(ar, ai, reverse):
    p1 = (ar, ai)
    p2 = _cmul(*p1, *p1)
    p4 = _cmul(*p2, *p2)
    p8 = _cmul(*p4, *p4)
    p3 = _cmul(*p2, *p1)
    p5 = _cmul(*p4, *p1)
    p6 = _cmul(*p4, *p2)
    p7 = _cmul(*p4, *p3)
    pows = [p1, p2, p3, p4, p5, p6, p7, p8]
    rows = lax.broadcasted_iota(jnp.int32, (8, ar.shape[1]), 0)
    tr = jnp.zeros((8, ar.shape[1]), F32)
    ti = jnp.zeros((8, ar.shape[1]), F32)
    for r in range(8):
        pw = pows[7 - r] if reverse else pows[r]
        tr = jnp.where(rows == r, pw[0], tr)
        ti = jnp.where(rows == r, pw[1], ti)
    return p1, p2, p4, p8, tr, ti


def _tile_scan(xr, xi, p1, p2, p4, reverse):
    rows = lax.broadcasted_iota(jnp.int32, xr.shape, 0)
    for s, (pr, pi) in ((1, p1), (2, p2), (4, p4)):
        if reverse:
            keep = rows < 8 - s
            sr, si = pltpu.roll(xr, 8 - s, 0), pltpu.roll(xi, 8 - s, 0)
        else:
            keep = rows >= s
            sr, si = pltpu.roll(xr, s, 0), pltpu.roll(xi, s, 0)
        sr, si = jnp.where(keep, sr, 0.0), jnp.where(keep, si, 0.0)
        mr, mi = _cmul(pr, pi, sr, si)
        xr, xi = xr + mr, xi + mi
    return xr, xi


def _s5_scan_fwd(bu, a, name, tb=512, comm=None):
    t = bu.shape[0]
    cb = SCAN_CB
    nt = t // tb

    def body(b_ref, a_ref, x_ref, carry):
        @pl.when(pl.program_id(1) == 0)
        def _():
            carry[...] = jnp.zeros_like(carry)

        ar, ai = a_ref[:, 0:cb], a_ref[:, cb:2 * cb]
        p1, p2, p4, p8, tr, ti = _scan_tables(ar, ai, False)

        def step(j, c):
            cr, ci = c
            i = pl.multiple_of(j * 8, 8)
            xr, xi = _tile_scan(b_ref[pl.ds(i, 8), 0:cb], b_ref[pl.ds(i, 8), cb:2 * cb], p1, p2, p4, False)
            mr, mi = _cmul(tr, ti, cr, ci)
            xr, xi = xr + mr, xi + mi
            x_ref[pl.ds(i, 8), 0:cb] = xr
            x_ref[pl.ds(i, 8), cb:2 * cb] = xi
            return xr[7:8, :], xi[7:8, :]

        cr, ci = lax.fori_loop(0, tb // 8, step, (carry[0:1, :], carry[1:2, :]), unroll=2)
        carry[0:1, :] = cr
        carry[1:2, :] = ci

    blk = pl.BlockSpec((tb, 2 * cb), lambda j, i: (i, j))
    return _call(
        body, name=name, grid=(SSM_CH // cb, nt),
        in_specs=[blk, pl.BlockSpec((1, 2 * cb), lambda j, i: (0, j))], out_specs=blk,
        out_shape=jax.ShapeDtypeStruct((t, 2 * SSM_CH), F32), scratch_shapes=[pltpu.VMEM((8, cb), F32)],
        sem=("parallel", "arbitrary"), args=(bu, a), comm=comm)


def _s5_scan_bwd(dx, x, a, name, tb=512, comm=None):
    t = dx.shape[0]
    cb = SCAN_CB
    nt = t // tb
    nj = tb // 8

    def body(d_ref, x_ref, xp_ref, a_ref, l_ref, da_ref, carry, acc):
        tblk = pl.program_id(1)

        @pl.when(tblk == 0)
        def _():
            carry[...] = jnp.zeros_like(carry)
            acc[...] = jnp.zeros_like(acc)

        ar, ai = a_ref[:, 0:cb], a_ref[:, cb:2 * cb]
        p1, p2, p4, p8, tr, ti = _scan_tables(ar, -ai, True)
        rows = lax.broadcasted_iota(jnp.int32, (8, cb), 0)

        def step(jj, c):
            cr, ci, sr_acc, si_acc = c
            j = nj - 1 - jj
            i = pl.multiple_of(j * 8, 8)
            lr, li = _tile_scan(d_ref[pl.ds(i, 8), 0:cb], d_ref[pl.ds(i, 8), cb:2 * cb], p1, p2, p4, True)
            mr, mi = _cmul(tr, ti, cr, ci)
            lr, li = lr + mr, li + mi
            l_ref[pl.ds(i, 8), 0:cb] = lr
            l_ref[pl.ds(i, 8), cb:2 * cb] = li
            ip = pl.multiple_of(jnp.maximum(j - 1, 0) * 8, 8)
            prev_r = jnp.where(j > 0, x_ref[pl.ds(ip, 8), 0:cb], xp_ref[:, 0:cb])
            prev_i = jnp.where(j > 0, x_ref[pl.ds(ip, 8), cb:2 * cb], xp_ref[:, cb:2 * cb])
            edge = jnp.where(jnp.logical_and(j == 0, tblk == nt - 1), 0.0, 1.0)
            xs_r = jnp.where(rows == 0, pltpu.roll(prev_r, 1, 0) * edge, pltpu.roll(x_ref[pl.ds(i, 8), 0:cb], 1, 0))
            xs_i = jnp.where(rows == 0, pltpu.roll(prev_i, 1, 0) * edge, pltpu.roll(x_ref[pl.ds(i, 8), cb:2 * cb], 1, 0))
            sr_acc = sr_acc + lr * xs_r + li * xs_i
            si_acc = si_acc + li * xs_r - lr * xs_i
            return lr[0:1, :], li[0:1, :], sr_acc, si_acc

        cr, ci, sr_acc, si_acc = lax.fori_loop(
            0, nj, step, (carry[0:1, :], carry[1:2, :], acc[:, 0:cb], acc[:, cb:2 * cb]))
        carry[0:1, :] = cr
        carry[1:2, :] = ci
        acc[:, 0:cb] = sr_acc
        acc[:, cb:2 * cb] = si_acc

        @pl.when(tblk == nt - 1)
        def _():
            da_ref[...] = jnp.sum(acc[...], axis=0, keepdims=True)

    blk = pl.BlockSpec((tb, 2 * cb), lambda j, i: (nt - 1 - i, j))
    prev = pl.BlockSpec((8, 2 * cb), lambda j, i: (jnp.maximum((nt - 1 - i) * (tb // 8) - 1, 0), j))
    vec = pl.BlockSpec((1, 2 * cb), lambda j, i: (0, j))
    return _call(
        body, name=name, grid=(SSM_CH // cb, nt), in_specs=[blk, blk, prev, vec], out_specs=[blk, vec],
        out_shape=[jax.ShapeDtypeStruct((t, 2 * SSM_CH), F32), jax.ShapeDtypeStruct((1, 2 * SSM_CH), F32)],
        scratch_shapes=[pltpu.VMEM((8, cb), F32), pltpu.VMEM((8, 2 * cb), F32)],
        sem=("parallel", "arbitrary"), args=(dx, x, x, a), comm=comm)


def _glu_fwd(yc, u, dvec, wg, bg, name, tr=256):
    t = yc.shape[0]

    def body(yc_ref, u_ref, d_ref, w_ref, b_ref, yl_ref, yb_ref):
        yl = yc_ref[...] + d_ref[...] * u_ref[...]
        yl_ref[...] = yl
        yg, _ = _gelu_and_grad(yl)
        z = jnp.dot(yg.astype(BF16), w_ref[...], preferred_element_type=F32) + b_ref[...]
        yb_ref[...] = (yg * _sigmoid(z)).astype(BF16)

    blk = pl.BlockSpec((tr, SSM_WIDTH), lambda i: (i, 0))
    vec = pl.BlockSpec((1, SSM_WIDTH), lambda i: (0, 0))
    return pl.pallas_call(
        body, name=name, grid=(t // tr,),
        in_specs=[blk, blk, vec, pl.BlockSpec((SSM_WIDTH, SSM_WIDTH), lambda i: (0, 0)), vec],
        out_specs=[blk, blk],
        out_shape=[jax.ShapeDtypeStruct((t, SSM_WIDTH), F32), jax.ShapeDtypeStruct((t, SSM_WIDTH), BF16)],
        compiler_params=_params(("parallel",)))(yc, u, dvec, wg, bg)


def _glu_bwd(yl, u, dvec, wg, bg, dyb, name, tr=256):
    t = yl.shape[0]

    def body(yl_ref, u_ref, d_ref, w_ref, b_ref, dy_ref, dyl_ref, du_ref, dw_ref, db_ref, dd_ref):
        @pl.when(pl.program_id(0) == 0)
        def _():
            dw_ref[...] = jnp.zeros_like(dw_ref)
            db_ref[...] = jnp.zeros_like(db_ref)
            dd_ref[...] = jnp.zeros_like(dd_ref)

        ylv, dyv, wv = yl_ref[...], dy_ref[...].astype(F32), w_ref[...]
        yg, dgelu = _gelu_and_grad(ylv)
        ygb = yg.astype(BF16)
        z = jnp.dot(ygb, wv, preferred_element_type=F32) + b_ref[...]
        sg = _sigmoid(z)
        dz = dyv * yg * sg * (1.0 - sg)
        dzb = dz.astype(BF16)
        dyg = dyv * sg + lax.dot_general(dzb, wv, (((1,), (1,)), ((), ())), preferred_element_type=F32)
        dyl = dyg * dgelu
        dyl_ref[...] = dyl.astype(BF16)
        du_ref[...] = dyl * d_ref[...]
        dw_ref[...] += lax.dot_general(ygb, dzb, (((0,), (0,)), ((), ())), preferred_element_type=F32)
        db_ref[...] += jnp.sum(dz, axis=0, keepdims=True)
        dd_ref[...] += jnp.sum(dyl * u_ref[...], axis=0, keepdims=True)

    blk = pl.BlockSpec((tr, SSM_WIDTH), lambda i: (i, 0))
    vec = pl.BlockSpec((1, SSM_WIDTH), lambda i: (0, 0))
    wsp = pl.BlockSpec((SSM_WIDTH, SSM_WIDTH), lambda i: (0, 0))
    return pl.pallas_call(
        body, name=name, grid=(t // tr,), in_specs=[blk, blk, vec, wsp, vec, blk],
        out_specs=[blk, blk, wsp, vec, vec],
        out_shape=[jax.ShapeDtypeStruct((t, SSM_WIDTH), BF16), jax.ShapeDtypeStruct((t, SSM_WIDTH), F32),
                   jax.ShapeDtypeStruct((SSM_WIDTH, SSM_WIDTH), F32), jax.ShapeDtypeStruct((1, SSM_WIDTH), F32),
                   jax.ShapeDtypeStruct((1, SSM_WIDTH), F32)],
        compiler_params=_params(("arbitrary",)))(yl, u, dvec, wg, bg, dyb)


def _mesh_pos():
    return lax.axis_index("x"), lax.axis_index("y"), lax.axis_index("c")


def _device_index():
    x, y, c = _mesh_pos()
    return 4 * x + 2 * y + c


def _gather_comm(arrays):
    na = len(arrays)

    def own_copy(ins, outs, sems, ai):
        return pltpu.make_async_copy(ins[ai], outs[ai].at[_device_index()], sems[2].at[ai])

    def ctx(ins, outs, sems):
        send_sems, recv_sems = sems[:2]
        x, y, c = _mesh_pos()
        chips = [(1 - x, y), (x, 1 - y), (1 - x, 1 - y)]

        def copy(ai, kk, block, to, own=False):
            slot = outs[ai].at[4 * block[0] + 2 * block[1] + block[2]]
            return pltpu.make_async_remote_copy(
                src_ref=ins[ai] if own else slot, dst_ref=slot, send_sem=send_sems.at[ai, kk],
                recv_sem=recv_sems.at[ai, kk], device_id=to, device_id_type=MESH)

        return (x, y, c), (x, y, 1 - c), chips, c, copy

    def start(ins, outs, sems):
        me, sibling, chips, c, copy = ctx(ins, outs, sems)
        for ai in range(na):
            copy(ai, 0, me, sibling, own=True).start()
            for j, chip in enumerate(chips):
                copy(ai, 1 + j, me, (*chip, c), own=True).start()
        for ai in range(na):
            own_copy(ins, outs, sems, ai).start()

    def mid(ins, outs, sems):
        me, sibling, chips, c, copy = ctx(ins, outs, sems)
        for ai in range(na):
            for j, chip in enumerate(chips):
                copy(ai, 1 + j, (*chip, c), me).wait_recv()
                copy(ai, 4 + j, (*chip, c), sibling).start()

    def end(ins, outs, sems):
        me, sibling, chips, c, copy = ctx(ins, outs, sems)
        for ai in range(na):
            copy(ai, 0, sibling, me).wait_recv()
            copy(ai, 0, me, sibling, own=True).wait_send()
            for j, chip in enumerate(chips):
                copy(ai, 4 + j, (*chip, 1 - c), me).wait_recv()
                copy(ai, 1 + j, me, (*chip, c), own=True).wait_send()
                copy(ai, 4 + j, (*chip, c), sibling).wait_send()
            own_copy(ins, outs, sems, ai).wait()

    return Comm(arrays, [jax.ShapeDtypeStruct((N_DEV,) + a.shape, a.dtype) for a in arrays],
                [pltpu.SemaphoreType.DMA((na, 7)), pltpu.SemaphoreType.DMA((na, 7)), pltpu.SemaphoreType.DMA((na,))],
                start, end, mid)


def _swap_comm(arrays):
    na = len(arrays)
    offs = np.concatenate([[0], np.cumsum([a.shape[1] for a in arrays])]).astype(int)

    def copies(ins, outs, sems):
        x, y, c = _mesh_pos()
        return [pltpu.make_async_remote_copy(
            src_ref=ins[ai].at[2 * k + 1 - c], dst_ref=outs[0].at[k, pl.ds(int(offs[ai]), arrays[ai].shape[1])],
            send_sem=sems[0].at[ai, k], recv_sem=sems[1].at[ai, k], device_id=(x, y, 1 - c), device_id_type=MESH)
            for ai in range(na) for k in range(4)]

    def start(ins, outs, sems):
        for cp in copies(ins, outs, sems):
            cp.start()

    def end(ins, outs, sems):
        for cp in copies(ins, outs, sems):
            cp.wait()

    return Comm(arrays, [jax.ShapeDtypeStruct((4, int(offs[-1]), PACK_COLS), arrays[0].dtype)],
                [pltpu.SemaphoreType.DMA((na, 4)), pltpu.SemaphoreType.DMA((na, 4))], start, end)


def _chips_comm(send):
    def copies(ins, outs, sems):
        x, y, c = _mesh_pos()
        chips = [(1 - x, y), (x, 1 - y), (1 - x, 1 - y)]
        return [pltpu.make_async_remote_copy(
            src_ref=ins[0].at[2 * cx + cy], dst_ref=outs[0].at[j], send_sem=sems[0].at[j], recv_sem=sems[1].at[j],
            device_id=(cx, cy, c), device_id_type=MESH) for j, (cx, cy) in enumerate(chips)]

    def start(ins, outs, sems):
        for cp in copies(ins, outs, sems):
            cp.start()

    def end(ins, outs, sems):
        for cp in copies(ins, outs, sems):
            cp.wait()

    return Comm([send], [jax.ShapeDtypeStruct((3,) + send.shape[1:], send.dtype)],
                [pltpu.SemaphoreType.DMA((3,)), pltpu.SemaphoreType.DMA((3,))], start, end)


def _all_gather(arrays, name):
    na = len(arrays)

    def body(*refs):
        ins, outs = refs[:na], refs[na:2 * na]
        send_sems, recv_sems, local_sems = refs[2 * na:]
        x, y, c = _mesh_pos()
        me, sibling = (x, y, c), (x, y, 1 - c)
        chips = [(1 - x, y), (x, 1 - y), (1 - x, 1 - y)]
        waits = []
        for ai in range(na):
            in_ref, out_ref = ins[ai], outs[ai]

            def slot(px, py, pc, out_ref=out_ref):
                return out_ref.at[4 * px + 2 * py + pc]

            def copy(kk, block, to, src=None, ai=ai, slot=slot):
                return pltpu.make_async_remote_copy(
                    src_ref=slot(*block) if src is None else src, dst_ref=slot(*block),
                    send_sem=send_sems.at[ai, kk], recv_sem=recv_sems.at[ai, kk], device_id=to, device_id_type=MESH)

            mine = pltpu.make_async_copy(in_ref, slot(*me), local_sems.at[ai])
            mine.start()
            first = [copy(0, me, sibling, src=in_ref)]
            first += [copy(1 + j, me, (*chip, c), src=in_ref) for j, chip in enumerate(chips)]
            for cp in first:
                cp.start()
            waits.append((copy, mine, first))
        sends = []
        for ai in range(na):
            copy, mine, first = waits[ai]
            passed = [copy(4 + j, (*chip, c), sibling) for j, chip in enumerate(chips)]
            for j, chip in enumerate(chips):
                copy(1 + j, (*chip, c), me).wait_recv()
                passed[j].start()
            sends.append(passed)
        for ai in range(na):
            copy, mine, first = waits[ai]
            copy(0, sibling, me).wait_recv()
            for j, chip in enumerate(chips):
                copy(4 + j, (*chip, 1 - c), me).wait_recv()
            for cp in first + sends[ai]:
                cp.wait_send()
            mine.wait()

    any_spec = pl.BlockSpec(memory_space=pl.ANY)
    return pl.pallas_call(
        body, name=name, in_specs=[any_spec] * na, out_specs=[any_spec] * na,
        out_shape=[jax.ShapeDtypeStruct((N_DEV,) + a.shape, a.dtype) for a in arrays],
        scratch_shapes=[pltpu.SemaphoreType.DMA((na, 7)), pltpu.SemaphoreType.DMA((na, 7)),
                        pltpu.SemaphoreType.DMA((na,))],
        compiler_params=pltpu.CompilerParams(has_side_effects=True))(*arrays)


def _swap_sibling(arrays, name):
    na = len(arrays)
    offs = np.concatenate([[0], np.cumsum([a.shape[1] for a in arrays])]).astype(int)
    rows = int(offs[-1])

    def body(*refs):
        ins, recv_ref = refs[:na], refs[na]
        send_sems, recv_sems = refs[na + 1:]
        x, y, c = _mesh_pos()
        started = []
        for ai in range(na):
            span = pl.ds(int(offs[ai]), arrays[ai].shape[1])
            for k in range(4):
                remote = pltpu.make_async_remote_copy(
                    src_ref=ins[ai].at[2 * k + 1 - c], dst_ref=recv_ref.at[k, span], send_sem=send_sems.at[ai, k],
                    recv_sem=recv_sems.at[ai, k], device_id=(x, y, 1 - c), device_id_type=MESH)
                remote.start()
                started.append(remote)
        for remote in started:
            remote.wait()

    any_spec = pl.BlockSpec(memory_space=pl.ANY)
    return pl.pallas_call(
        body, name=name, in_specs=[any_spec] * na, out_specs=any_spec,
        out_shape=jax.ShapeDtypeStruct((4, rows, PACK_COLS), arrays[0].dtype),
        scratch_shapes=[pltpu.SemaphoreType.DMA((na, 4)), pltpu.SemaphoreType.DMA((na, 4))])(*arrays)


def _exchange_chips(send, name):
    def body(s_ref, o_ref, send_sems, recv_sems):
        x, y, c = _mesh_pos()
        chips = [(1 - x, y), (x, 1 - y), (1 - x, 1 - y)]
        cps = [pltpu.make_async_remote_copy(
            src_ref=s_ref.at[2 * cx + cy], dst_ref=o_ref.at[j], send_sem=send_sems.at[j], recv_sem=recv_sems.at[j],
            device_id=(cx, cy, c), device_id_type=MESH) for j, (cx, cy) in enumerate(chips)]
        for cp in cps:
            cp.start()
        for cp in cps:
            cp.wait()

    any_spec = pl.BlockSpec(memory_space=pl.ANY)
    return pl.pallas_call(
        body, name=name, in_specs=[any_spec], out_specs=any_spec,
        out_shape=jax.ShapeDtypeStruct((3,) + send.shape[1:], send.dtype),
        scratch_shapes=[pltpu.SemaphoreType.DMA((3,)), pltpu.SemaphoreType.DMA((3,))])(send)


def _pair_sum(keep, recv, name, tr=464):
    nchip, rows, cols = keep.shape

    def body(g_ref, r_ref, o_ref):
        o_ref[...] = (g_ref[...].astype(F32) + r_ref[...].astype(F32)).astype(BF16)

    blk = pl.BlockSpec((1, tr, cols), lambda k, i: (k, i, 0))
    return pl.pallas_call(
        body, name=name, grid=(nchip, rows // tr), in_specs=[blk, blk], out_specs=blk,
        out_shape=jax.ShapeDtypeStruct((nchip, rows, cols), BF16),
        compiler_params=_params(("parallel", "parallel")))(keep, recv)


def _chip_sum(own, others, name, tr=464):
    _, rows, cols = own.shape
    chip = (2 * lax.axis_index("x") + lax.axis_index("y")).astype(jnp.int32).reshape(1)

    def body(chip_ref, own_ref, oth_ref, o_ref):
        del chip_ref
        acc = own_ref[0].astype(F32)
        for j in range(3):
            acc = acc + oth_ref[j].astype(F32)
        o_ref[...] = acc

    grid_spec = pltpu.PrefetchScalarGridSpec(
        num_scalar_prefetch=1, grid=(rows // tr,),
        in_specs=[pl.BlockSpec((1, tr, cols), lambda i, chip_ref: (chip_ref[0], i, 0)),
                  pl.BlockSpec((3, tr, cols), lambda i, chip_ref: (0, i, 0))],
        out_specs=pl.BlockSpec((tr, cols), lambda i, chip_ref: (i, 0)))
    return pl.pallas_call(
        body, name=name, grid_spec=grid_spec, out_shape=jax.ShapeDtypeStruct((rows, cols), F32),
        compiler_params=_params(("parallel",)))(chip, own, others)


def _sum_leading(parts, name, tr=464):
    nparts, rows, cols = parts.shape
    tr = tr if rows % tr == 0 else rows

    def body(p_ref, o_ref):
        acc = p_ref[0].astype(F32)
        for i in range(1, nparts):
            acc = acc + p_ref[i].astype(F32)
        o_ref[...] = acc

    return pl.pallas_call(
        body, name=name, grid=(rows // tr,),
        in_specs=[pl.BlockSpec((nparts, tr, cols), lambda i: (0, i, 0))],
        out_specs=pl.BlockSpec((tr, cols), lambda i: (i, 0)), out_shape=jax.ShapeDtypeStruct((rows, cols), F32),
        compiler_params=_params(("parallel",)))(parts)


def _adamw(w, g, m, v, name):
    shape = w.shape
    cols = shape[-1]
    rows = int(np.prod(shape[:-1])) if len(shape) > 1 else 1
    w2, g2, m2, v2 = (a.reshape(rows, cols) for a in (w, g, m, v))
    tr = rows
    for cand in (512, 256, 128, 64, 32, 16, 8):
        if rows % cand == 0 and rows > cand:
            tr = cand
            break
    bc1, bc2 = 1.0 - ADAM_B1 ** ADAM_STEP, 1.0 - ADAM_B2 ** ADAM_STEP

    def body(w_ref, g_ref, m_ref, v_ref, d_ref, nm_ref, nv_ref):
        gv = g_ref[...]
        nm = ADAM_B1 * m_ref[...] + (1.0 - ADAM_B1) * gv
        nv = ADAM_B2 * v_ref[...] + (1.0 - ADAM_B2) * (gv * gv)
        nm_ref[...] = nm
        nv_ref[...] = nv
        d_ref[...] = -ADAM_LR * ((nm / bc1) / (jnp.sqrt(nv / bc2) + ADAM_EPS) + ADAM_WD * w_ref[...])

    blk = pl.BlockSpec((tr, cols), lambda i: (i, 0))
    outs = pl.pallas_call(
        body, name=name, grid=(rows // tr,), in_specs=[blk] * 4, out_specs=[blk] * 3,
        out_shape=[jax.ShapeDtypeStruct((rows, cols), F32)] * 3, compiler_params=_params(("parallel",)))(w2, g2, m2, v2)
    return tuple(o.reshape(shape) for o in outs)


WEIGHT_NAMES = ['norm_mix_g', 'norm_xa_g', 'norm_ffn_g', 'norm_mem_g', 'norm_final_g', 'w_in_ab', 'conv_qkv_a',
                'a_log_a', 'dt_bias_a', 'onorm_g_a', 'ssm_lambda_re', 'ssm_lambda_im', 'ssm_b_re', 'ssm_b_im',
                'ssm_c_re', 'ssm_c_im', 'ssm_d', 'ssm_log_dt', 'w_glu_b', 'b_glu_b', 'w_out_ab', 'pool_w',
                'pool_scale', 'xa_wq', 'xa_wkv', 'xa_wo', 'ffn_w_up', 'ffn_conv', 'ffn_w_down']
BIG_SHARDED = {'w_in_ab': ((1, 1024, 2568), 2), 'w_glu_b': ((1, 512, 512), 1), 'w_out_ab': ((1, 1024, 1024), 1),
               'pool_w': ((1, 4, 256, 256), 2), 'xa_wq': ((2, 1024, 1024), 1), 'xa_wkv': ((2, 1024, 2048), 2),
               'xa_wo': ((2, 1024, 1024), 1), 'ffn_w_up': ((2, 1024, 5632), 2), 'ffn_w_down': ((2, 2816, 1024), 1)}
SMALL_SHARDED = {'conv_qkv_a': ((1, 4, 1536), 2), 'pool_scale': ((1, 1024), 1), 'ffn_conv': ((2, 3, 5632), 2)}
REPLICATED = {'norm_mix_g': (2, 1024), 'norm_xa_g': (2, 1024), 'norm_ffn_g': (2, 1024), 'norm_mem_g': (1024,),
              'norm_final_g': (1024,), 'a_log_a': (1, 4), 'dt_bias_a': (1, 4), 'onorm_g_a': (1, 128),
              'ssm_lambda_re': (1, 32, 64), 'ssm_lambda_im': (1, 32, 64), 'ssm_b_re': (1, 32, 64, 16),
              'ssm_b_im': (1, 32, 64, 16), 'ssm_c_re': (1, 32, 16, 64), 'ssm_c_im': (1, 32, 16, 64),
              'ssm_d': (1, 32, 16), 'ssm_log_dt': (1, 32), 'b_glu_b': (1, 512)}
PACK_ROW_ALIGN = 8


def _shard_shape(shape, axis):
    return tuple(s // N_DEV if i == axis else s for i, s in enumerate(shape))


def _round_up(n, m):
    return (n + m - 1) // m * m


def _pack(arrays):
    total = sum(int(np.prod(a.shape)) for a in arrays)
    padded = _round_up(total, PACK_COLS * PACK_ROW_ALIGN)
    parts = [a.astype(F32).reshape(-1) for a in arrays]
    if padded != total:
        parts.append(jnp.zeros((padded - total,), F32))
    return jnp.concatenate(parts).reshape(padded // PACK_COLS, PACK_COLS)


def _unpack(packed, shapes):
    flat, out, off = packed.reshape(-1), [], 0
    for shape in shapes:
        size = int(np.prod(shape))
        out.append(flat[off:off + size].reshape(shape))
        off += size
    return out


def _split_shards(full, axis):
    shape = full.shape
    s = shape[axis] // N_DEV
    a = full.reshape(shape[:axis] + (N_DEV, s) + shape[axis + 1:])
    return jnp.moveaxis(a, axis, 0).reshape(N_DEV, -1)


def _merge_shards(pieces, shape, axis):
    sh = _shard_shape(shape, axis)
    a = pieces.reshape((N_DEV,) + sh)
    a = jnp.moveaxis(a, 0, axis)
    return a.reshape(shape)


_SCAN_NB = SSM_CH // SCAN_CB


def _to_scan_layout(m, axis):
    shape = m.shape
    m = m.reshape(shape[:axis] + (2, _SCAN_NB, SCAN_CB) + shape[axis + 1:])
    return jnp.swapaxes(m, axis, axis + 1).reshape(shape)


def _from_scan_layout(m, axis):
    shape = m.shape
    m = m.reshape(shape[:axis] + (_SCAN_NB, 2, SCAN_CB) + shape[axis + 1:])
    return jnp.swapaxes(m, axis, axis + 1).reshape(shape)


def _s5_discretise(lam_re, lam_im, b_re, b_im, log_dt):
    dt = jnp.exp(log_dt)[:, None]
    mag = jnp.exp(lam_re * dt)
    ang = lam_im * dt
    lb_re, lb_im = mag * jnp.cos(ang), mag * jnp.sin(ang)
    den = lam_re * lam_re + lam_im * lam_im
    nr, ni = lb_re - 1.0, lb_im
    coef_re = (nr * lam_re + ni * lam_im) / den
    coef_im = (ni * lam_re - nr * lam_im) / den
    bb_re = coef_re[..., None] * b_re - coef_im[..., None] * b_im
    bb_im = coef_re[..., None] * b_im + coef_im[..., None] * b_re
    return lb_re, lb_im, bb_re, bb_im


def _s5_matrices(lb_re, lb_im, bb_re, bb_im, c_re, c_im):
    eye = jnp.eye(N_GROUPS, dtype=F32)
    bmat = lambda bb: jnp.einsum('gph,gk->ghkp', bb, eye).reshape(SSM_WIDTH, SSM_CH)
    cmat = lambda cc: jnp.einsum('ghp,gk->gpkh', cc, eye).reshape(SSM_CH, SSM_WIDTH)
    b_in = _to_scan_layout(jnp.concatenate([bmat(bb_re), bmat(bb_im)], axis=1), 1)
    c_out = _to_scan_layout(jnp.concatenate([cmat(c_re), -cmat(c_im)], axis=0), 0)
    a_row = _to_scan_layout(jnp.concatenate([lb_re.reshape(1, SSM_CH), lb_im.reshape(1, SSM_CH)], axis=1), 1)
    return b_in, c_out, a_row


def _s5_matrix_grads(db_in, dc_out, da_row):
    db_nat = _from_scan_layout(db_in, 1)
    dc_nat = _from_scan_layout(dc_out, 0)
    da_nat = _from_scan_layout(da_row, 1)
    eye = jnp.eye(N_GROUPS, dtype=F32)
    bgrad = lambda m: jnp.einsum('ghkp,gk->gph', m.reshape(N_GROUPS, SSM_GROUP, N_GROUPS, SSM_STATE), eye)
    cgrad = lambda m: jnp.einsum('gpkh,gk->ghp', m.reshape(N_GROUPS, SSM_STATE, N_GROUPS, SSM_GROUP), eye)
    dbb_re, dbb_im = bgrad(db_nat[:, :SSM_CH]), bgrad(db_nat[:, SSM_CH:])
    dc_re, dc_im = cgrad(dc_nat[:SSM_CH]), -cgrad(dc_nat[SSM_CH:])
    dlb_re = da_nat[0, :SSM_CH].reshape(N_GROUPS, SSM_STATE)
    dlb_im = da_nat[0, SSM_CH:].reshape(N_GROUPS, SSM_STATE)
    return dlb_re, dlb_im, dbb_re, dbb_im, dc_re, dc_im


def _as_pieces(a):
    return a.reshape(N_DEV, a.shape[0] // N_DEV, a.shape[1])


def _hybrid_fwd(xn, x, wts, p, weights, riders):
    sv = {}
    hq = _mm(xn, wts['w_qkv_t'], "nt", "l0_in_qkv")
    gate = _mm(xn, wts['w_gate_t'], "nt", "l0_in_gate")
    ba = _mm(xn, wts['w_ba_t'], "nt", "l0_in_ba")
    u = _mm(xn, wts['w_u_t'], "nt", "l0_in_u")
    conv = p['conv_qkv']
    q = _qkv_pre_fwd(hq, conv, 0, 4, True, HEAD_A ** -0.5, "l0_q_pre")
    k = _qkv_pre_fwd(hq, conv, 4, 4, True, 1.0, "l0_k_pre")
    v = _qkv_pre_fwd(hq, conv, 8, 4, False, 1.0, "l0_v_pre")
    gates = _gates_fwd(ba, p['arow'], p['brow'], "l0_gates")
    o, tm_all, s_all = riders.run("l0_gdr_fwd", _gdr_fwd, q, k, v, gates)
    wts['w_glu'], wts['w_out'] = weights.full['w_glu'], weights.full['w_out']
    y_a = _onorm_fwd(o, gate, p['onorm_g'], "l0_onorm")
    bu = riders.run("l0_s5_bu", _mm, u, p['b_in'], "nn")
    xs = riders.run("l0_s5_scan", _s5_scan_fwd, bu, p['a_row'])
    yc = riders.run("l0_s5_cx", _mm, xs, p['c_out'], "nn")
    yl, y_b = _glu_fwd(yc, u, p['d_row'], wts['w_glu'], p['b_glu'], "l0_glu")
    mixed = jnp.concatenate([y_a, y_b], axis=1)
    x1 = _mm(mixed, wts['w_out'], "nn", "l0_out", res=x)
    sv.update(hq=hq, gate=gate, ba=ba, u=u, q=q, k=k, v=v, gb=gates, o=o, tm=tm_all, s=s_all, xs=xs, yl=yl, mixed=mixed)
    return x1, sv


def _hybrid_bwd(dx1, xn, wts, p, sv, riders):
    gr = {}
    dmixed = _mm(dx1, wts['w_out'], "nt", "l0_out_dx", out_dtype=BF16)
    riders.grad('w_out', _as_pieces(_mm(sv['mixed'], dx1, "tn", "l0_out_dw", out_dtype=BF16)))
    dya, dyb = dmixed[:, :WIDTH_A], dmixed[:, WIDTH_A:]
    dyl, du_direct, dw_glu, gr['b_glu_b'], dd = _glu_bwd(
        sv['yl'], sv['u'], p['d_row'], wts['w_glu'], p['b_glu'], dyb, "l0_glu_bwd")
    riders.grad('w_glu', dw_glu.astype(BF16).reshape(N_DEV, -1, PACK_COLS))
    dxs = riders.run("l0_s5_cx_dx", _mm, dyl, p['c_out'], "nt")
    dc_out = _mm(sv['xs'], dyl, "tn", "l0_s5_cx_dw")
    lam, da_row = riders.run("l0_s5_scan_bwd", _s5_scan_bwd, dxs, sv['xs'], p['a_row'])
    du = _mm(lam, p['b_in'], "nt", "l0_s5_bu_dx", res=du_direct, out_dtype=BF16)
    db_in = _mm(sv['u'], lam, "tn", "l0_s5_bu_dw")
    gr['s5'] = (db_in, dc_out, da_row, dd)
    do, dgate, gr['onorm_g_a'] = _onorm_bwd(sv['o'], sv['gate'], p['onorm_g'], dya, "l0_onorm_bwd")
    dq, dk, dv, dgb = riders.run("l0_gdr_bwd", _gdr_bwd, sv['q'], sv['k'], sv['v'], sv['gb'], sv['tm'], sv['s'], do)
    conv = p['conv_qkv']
    dhq_q, dcw_q = _qkv_pre_bwd(sv['hq'], conv, dq, 0, 4, True, HEAD_A ** -0.5, "l0_q_pre_bwd")
    dhq_k, dcw_k = _qkv_pre_bwd(sv['hq'], conv, dk, 4, 4, True, 1.0, "l0_k_pre_bwd")
    dhq_v, dcw_v = _qkv_pre_bwd(sv['hq'], conv, dv, 8, 4, False, 1.0, "l0_v_pre_bwd")
    gr['conv_qkv_a'] = jnp.concatenate([dcw_q, dcw_k, dcw_v], axis=1)
    dhq = jnp.concatenate([dhq_q, dhq_k, dhq_v], axis=1)
    dba, da_log, ddt_bias = _gates_bwd(sv['ba'], p['arow'], p['brow'], dgb, "l0_gates_bwd")
    gr['a_log_a'], gr['dt_bias_a'] = da_log[:, 4:8], ddt_bias[:, 4:8]
    dxn = _mm(dhq, wts['w_qkv_t'], "nn", "l0_in_qkv_dx")
    dxn = _mm(dgate, wts['w_gate_t'], "nn", "l0_in_gate_dx", res=dxn)
    dxn = _mm(dba, wts['w_ba_t'], "nn", "l0_in_ba_dx", res=dxn)
    dxn = _mm(du, wts['w_u_t'], "nn", "l0_in_u_dx", res=dxn)
    dw_qkv_t = _mm(dhq, xn, "tn", "l0_in_qkv_dw", out_dtype=BF16)
    dw_gate_t = _mm(dgate, xn, "tn", "l0_in_gate_dw", out_dtype=BF16)
    dw_ba_t = _mm(dba, xn, "tn", "l0_in_ba_dw", out_dtype=BF16)
    dw_u_t = _mm(du, xn, "tn", "l0_in_u_dw", out_dtype=BF16)
    dw_in_t = _as_pieces(jnp.concatenate([dw_qkv_t, dw_gate_t, dw_ba_t[:8], dw_u_t], axis=0))
    riders.grad('w_in_t', jnp.concatenate(
        [dw_in_t, jnp.zeros((N_DEV, dict(PIECES)['w_in_t'] - W_IN_PIECE, D_MODEL), BF16)], axis=1))
    return dxn, gr


def _xa_fwd(x1, g, mem_n, wq, wkv_t, wo, tag, riders):
    xq = _rms_fwd(x1, g, BF16, tag + "_norm")
    q = _mm(xq, wq, "nn", tag + "_q", out_dtype=BF16)
    kv = _mm(mem_n, wkv_t, "nt", tag + "_kv", out_dtype=BF16)
    o = riders.run(tag + "_attn", _attn_fwd, q, kv)
    x2 = _mm(o, wo, "nn", tag + "_o", res=x1)
    return x2, dict(xq=xq, q=q, kv=kv, o=o)


def _xa_bwd(dx2, x1, g, mem_n, wq, wkv_t, wo, sv, tag, layer, riders):
    do = _mm(dx2, wo, "nt", tag + "_o_dx", out_dtype=BF16)
    riders.grad('wo%d' % layer, _as_pieces(_mm(sv['o'], dx2, "tn", tag + "_o_dw", out_dtype=BF16)))
    dq, dk, dv = _attn_bwd(sv['q'], sv['kv'], do, tag + "_attn_bwd")
    dkv = jnp.concatenate([dk, dv], axis=1).astype(BF16)
    dxq = _mm(dq, wq, "nt", tag + "_q_dx")
    riders.grad('wq%d' % layer, _as_pieces(_mm(sv['xq'], dq, "tn", tag + "_q_dw", out_dtype=BF16)))
    dmem_n = _mm(dkv, wkv_t, "nn", tag + "_kv_dx")
    riders.grad('wkv_t%d' % layer, _as_pieces(_mm(dkv, mem_n, "tn", tag + "_kv_dw", out_dtype=BF16)))
    dx1, dg = riders.run(tag + "_norm_bwd", _rms_bwd, x1, g, dxq, dx2)
    return dx1, dmem_n, dg


def _ffn_fwd(x2, g, w_up_t, conv, w_down, tag, riders):
    xf = _rms_fwd(x2, g, BF16, tag + "_norm")
    h = riders.run(tag + "_up", _mm, xf, w_up_t, "nt")
    a = riders.run(tag + "_act", _ffn_act_fwd, h, conv)
    x3 = _mm(a, w_down, "nn", tag + "_down", res=x2)
    return x3, dict(xf=xf, h=h, a=a)


def _ffn_bwd(dx3, x2, g, w_up_t, conv, w_down, sv, tag, layer, riders):
    da = _mm(dx3, w_down, "nt", tag + "_down_dx")
    riders.grad('down%d' % layer, _as_pieces(_mm(sv['a'], dx3, "tn", tag + "_down_dw", out_dtype=BF16)))
    dh, dconv = riders.run(tag + "_act_bwd", _ffn_act_bwd, sv['h'], conv, da)
    dxf = riders.run(tag + "_up_dx", _mm, dh, w_up_t, "nn")
    dw_up_t = riders.run(tag + "_up_dw", _mm, dh, sv['xf'], "tn", out_dtype=BF16)
    riders.grad('up_t%d' % layer, _as_pieces(dw_up_t))
    dx2, dg = riders.run(tag + "_norm_bwd", _rms_bwd, x2, g, dxf, dx3)
    return dx2, dconv, dg


BIG_NAMES, SMALL_NAMES, REP_NAMES = list(BIG_SHARDED), list(SMALL_SHARDED), list(REPLICATED)
BIG_SIZES = [int(np.prod(_shard_shape(*BIG_SHARDED[n]))) for n in BIG_NAMES]
SMALL_SIZES = [int(np.prod(_shard_shape(*SMALL_SHARDED[n]))) for n in SMALL_NAMES]


PIECES = [('w_in_t', 384), ('w_glu', 32), ('w_out', 128), ('pool_w', 32), ('wq0', 128), ('wq1', 128),
          ('wkv_t0', 256), ('wkv_t1', 256), ('wo0', 128), ('wo1', 128), ('up_t0', 704), ('up_t1', 704),
          ('down0', 352), ('down1', 352)]
PIECE_OFFS = dict(zip([k for k, _ in PIECES], np.concatenate([[0], np.cumsum([r for _, r in PIECES])[:-1]]).tolist()))
W_IN_ROWS = 4 * WIDTH_A + 2 * N_HEADS_A + SSM_WIDTH
W_IN_PIECE = W_IN_ROWS // N_DEV


def _row_tile(rows):
    return max(t for t in range(16, min(rows, 512) + 1, 16) if rows % t == 0)


class _Riders:
    def __init__(self):
        self.waiting = {}
        self.grads = {}
        self.groups = []
        self.reduced = {}

    def add(self, host, comm, then):
        self.waiting.setdefault(host, []).append((comm, then))

    def run(self, name, fn, *args, **kw):
        riders = self.waiting.pop(name, [])
        if not riders:
            return fn(*args, name=name, **kw)
        out, couts = fn(*args, name=name, comm=[c for c, _ in riders], **kw)
        for (_, then), got in zip(riders, couts):
            then(got)
        return out

    def exchange(self, comm, host, name, then):
        if host is None:
            then(_comm_only(comm, name))
        else:
            self.add(host, comm, then)

    def grad(self, key, pieces):
        self.grads[key] = pieces
        for group in [g for g in self.groups if all(k in self.grads for k in g[1])]:
            self.groups.remove(group)
            self._reduce(*group)

    def _reduce(self, name, keys, swap_host, chips_host):
        arrays = [self.grads[k] for k in keys]
        rows = sum(a.shape[1] for a in arrays)
        tile = _row_tile(rows)

        def after_chips(chip_sums, got):
            total = _chip_sum(chip_sums, got[0], name + "_chip_sum", tr=tile)
            off = 0
            for k, a in zip(keys, arrays):
                self.reduced[k] = total[off:off + a.shape[1]]
                off += a.shape[1]

        def after_swap(got):
            core = lax.axis_index("c")
            keep = jnp.concatenate(
                [lax.dynamic_index_in_dim(a.reshape(4, 2, a.shape[1], PACK_COLS), core, 1, keepdims=False)
                 for a in arrays], axis=1)
            chip_sums = _pair_sum(keep, got[0], name + "_pair_sum", tr=tile)
            self.exchange(_chips_comm(chip_sums), chips_host, name + "_to_chips",
                          functools.partial(after_chips, chip_sums))

        self.exchange(_swap_comm(arrays), swap_host, name + "_to_sibling", after_swap)


class _Weights:
    def __init__(self, inp):
        bf = lambda a: a.astype(BF16)
        local = {'w_in_t': bf(inp['w_in_ab'][0]).T, 'w_glu': bf(inp['w_glu_b'][0]), 'w_out': bf(inp['w_out_ab'][0]),
                 'pool_w': bf(inp['pool_w'][0]),
                 'small': _pack([inp[n] for n in SMALL_NAMES])}
        for l in range(2):
            local['wq%d' % l] = bf(inp['xa_wq'][l])
            local['wkv_t%d' % l] = bf(inp['xa_wkv'][l]).T
            local['wo%d' % l] = bf(inp['xa_wo'][l])
            local['up_t%d' % l] = bf(inp['ffn_w_up'][l]).T
            local['down%d' % l] = bf(inp['ffn_w_down'][l])
        self.local, self.full = local, {}

    def plan(self, keys):
        return _gather_comm([self.local[k] for k in keys])

    def land(self, keys, gathered):
        for k, g in zip(keys, gathered):
            if k == 'small':
                off = 0
                for n, size in zip(SMALL_NAMES, SMALL_SIZES):
                    self.full[n] = _merge_shards(g.reshape(N_DEV, -1)[:, off:off + size], *SMALL_SHARDED[n])
                    off += size
            elif k == 'pool_w':
                self.full[k] = jnp.swapaxes(g, 0, 1).reshape(len(POOL_WINDOWS), POOL_GROUP, POOL_GROUP)
            else:
                self.full[k] = g.reshape(N_DEV * g.shape[1], g.shape[2])


GATHER_FIRST = ['w_in_t', 'small']
GATHER_RIDES = [('l0_gdr_fwd', ['w_glu', 'w_out', 'wq0', 'wkv_t0', 'wo0', 'up_t0']),
                ('l0_s5_bu', ['pool_w', 'wq1']), ('l0_s5_scan', ['down0']), ('l0_s5_cx', ['wo1']),
                ('l0_xa_attn', ['wkv_t1']), ('l0_ffn_up', ['up_t1']), ('l0_ffn_act', ['down1'])]
GRAD_RIDES = [('g_down1', ['down1'], 'l1_ffn_act_bwd', 'l1_ffn_up_dx'),
              ('g_up1', ['up_t1'], 'l1_ffn_norm_bwd', 'l0_ffn_act_bwd'),
              ('g_xa1', ['wq1', 'wkv_t1', 'wo1', 'pool_w'], 'l1_mix_norm_bwd', 'l0_ffn_up_dx'),
              ('g_down0', ['down0'], 'l0_ffn_act_bwd', 'l0_ffn_up_dw'),
              ('g_up0', ['up_t0'], 'l0_ffn_norm_bwd', 'l0_gdr_bwd'),
              ('g_xa0', ['wq0', 'wkv_t0', 'wo0'], 'l0_xa_norm_bwd', 'l0_gdr_bwd'),
              ('g_out', ['w_out', 'w_glu'], 'l0_s5_cx_dx', 'l0_s5_scan_bwd'),
              ('g_in', ['w_in_t'], None, None)]


def _local_step(inp):
    f32_of = lambda n: inp[n].astype(F32)
    weights = _Weights(inp)
    riders = _Riders()
    riders.groups = list(GRAD_RIDES)
    full = weights.full
    weights.land(GATHER_FIRST, _comm_only(weights.plan(GATHER_FIRST), "gather_first"))
    for host, keys in GATHER_RIDES:
        riders.add(host, weights.plan(keys), functools.partial(weights.land, keys))
    w_in_t = full['w_in_t']
    wts0 = dict(w_qkv_t=w_in_t[:3 * WIDTH_A], w_gate_t=w_in_t[3 * WIDTH_A:4 * WIDTH_A],
                w_ba_t=jnp.concatenate([w_in_t[4 * WIDTH_A:4 * WIDTH_A + 8], jnp.zeros((LANE - 8, D_MODEL), BF16)], 0),
                w_u_t=w_in_t[4 * WIDTH_A + 8:])
    lb_disc, disc_vjp = jax.vjp(_s5_discretise, f32_of('ssm_lambda_re')[0], f32_of('ssm_lambda_im')[0],
                                f32_of('ssm_b_re')[0], f32_of('ssm_b_im')[0], f32_of('ssm_log_dt')[0])
    b_in, c_out, a_row = _s5_matrices(*lb_disc, f32_of('ssm_c_re')[0], f32_of('ssm_c_im')[0])
    zeros4 = jnp.zeros((1, 4), F32)
    p0 = dict(conv_qkv=full['conv_qkv_a'][0], onorm_g=f32_of('onorm_g_a'),
              arow=jnp.concatenate([zeros4, f32_of('a_log_a'), jnp.zeros((1, LANE - 8), F32)], 1),
              brow=jnp.concatenate([zeros4, f32_of('dt_bias_a'), jnp.zeros((1, LANE - 8), F32)], 1),
              b_in=b_in.astype(BF16), c_out=c_out.astype(BF16), a_row=a_row,
              d_row=f32_of('ssm_d').reshape(1, SSM_WIDTH), b_glu=f32_of('b_glu_b'))

    x0 = inp['x'][0]
    mem_n = _rms_fwd(inp['mem'][0], inp['norm_mem_g'], BF16, "mem_norm")
    xn0 = _rms_fwd(x0, inp['norm_mix_g'][0], BF16, "l0_mix_norm")
    x1, sv_mix0 = _hybrid_fwd(xn0, x0, wts0, p0, weights, riders)
    x2, sv_xa0 = _xa_fwd(x1, inp['norm_xa_g'][0], mem_n, full['wq0'], full['wkv_t0'], full['wo0'], "l0_xa", riders)
    x3, sv_ffn0 = _ffn_fwd(x2, inp['norm_ffn_g'][0], full['up_t0'], full['ffn_conv'][0], full['down0'], "l0_ffn", riders)
    xn1 = _rms_fwd(x3, inp['norm_mix_g'][1], F32, "l1_mix_norm")
    x4 = _pool_fwd(xn1, full['pool_w'], full['pool_scale'], x3, "l1_pool")
    x5, sv_xa1 = _xa_fwd(x4, inp['norm_xa_g'][1], mem_n, full['wq1'], full['wkv_t1'], full['wo1'], "l1_xa", riders)
    x6, sv_ffn1 = _ffn_fwd(x5, inp['norm_ffn_g'][1], full['up_t1'], full['ffn_conv'][1], full['down1'], "l1_ffn", riders)
    loss_part, dx6, dg_final = _loss_head(x6, inp['norm_final_g'], inp['loss_target'][0], "loss_head")

    dx5, dconv1, dg_ffn1 = _ffn_bwd(dx6, x5, inp['norm_ffn_g'][1], full['up_t1'], full['ffn_conv'][1], full['down1'],
                                    sv_ffn1, "l1_ffn", 1, riders)
    dx4, dmem1, dg_xa1 = _xa_bwd(dx5, x4, inp['norm_xa_g'][1], mem_n, full['wq1'], full['wkv_t1'], full['wo1'],
                                 sv_xa1, "l1_xa", 1, riders)
    dxn1, dpool_w, dpool_scale = _pool_bwd(xn1, full['pool_w'], full['pool_scale'], dx4, "l1_pool_bwd")
    pool_pieces = jnp.swapaxes(dpool_w.astype(BF16).reshape(len(POOL_WINDOWS), N_DEV, -1, POOL_GROUP), 0, 1)
    riders.grad('pool_w', pool_pieces.reshape(N_DEV, -1, PACK_COLS))
    dx3, dg_mix1 = riders.run("l1_mix_norm_bwd", _rms_bwd, x3, inp['norm_mix_g'][1], dxn1, dx4)
    dx2, dconv0, dg_ffn0 = _ffn_bwd(dx3, x2, inp['norm_ffn_g'][0], full['up_t0'], full['ffn_conv'][0], full['down0'],
                                    sv_ffn0, "l0_ffn", 0, riders)
    dx1, dmem0, dg_xa0 = _xa_bwd(dx2, x1, inp['norm_xa_g'][0], mem_n, full['wq0'], full['wkv_t0'], full['wo0'],
                                 sv_xa0, "l0_xa", 0, riders)
    dxn0, g_mix0 = _hybrid_bwd(dx1, xn0, wts0, p0, sv_mix0, riders)
    grad_x, dg_mix0 = _rms_bwd(x0, inp['norm_mix_g'][0], dxn0, dx1, "l0_mix_norm_bwd")
    _, dg_mem = _rms_bwd(inp['mem'][0], inp['norm_mem_g'], dmem0 + dmem1, None, "mem_norm_bwd")
    assert not riders.waiting and not riders.groups, (list(riders.waiting), riders.groups)

    db_in, dc_out, da_row, dd = g_mix0['s5']
    dlb_re, dlb_im, dbb_re, dbb_im, dc_re, dc_im = _s5_matrix_grads(db_in, dc_out, da_row)
    dlam_re, dlam_im, dbr, dbi, dlog_dt = disc_vjp((dlb_re, dlb_im, dbb_re, dbb_im))

    rep_grads = {
        'norm_mix_g': jnp.concatenate([dg_mix0, dg_mix1], 0), 'norm_xa_g': jnp.concatenate([dg_xa0, dg_xa1], 0),
        'norm_ffn_g': jnp.concatenate([dg_ffn0, dg_ffn1], 0), 'norm_mem_g': dg_mem.reshape(-1),
        'norm_final_g': dg_final.reshape(-1), 'a_log_a': g_mix0['a_log_a'], 'dt_bias_a': g_mix0['dt_bias_a'],
        'onorm_g_a': g_mix0['onorm_g_a'], 'ssm_lambda_re': dlam_re[None], 'ssm_lambda_im': dlam_im[None],
        'ssm_b_re': dbr[None], 'ssm_b_im': dbi[None], 'ssm_c_re': dc_re[None], 'ssm_c_im': dc_im[None],
        'ssm_d': dd.reshape(1, N_GROUPS, SSM_GROUP), 'ssm_log_dt': dlog_dt[None], 'b_glu_b': g_mix0['b_glu_b']}
    small_grads = {'conv_qkv_a': g_mix0['conv_qkv_a'][None], 'pool_scale': dpool_scale,
                   'ffn_conv': jnp.stack([dconv0, dconv1])}
    for key, rows in PIECES:
        assert riders.reduced[key].shape == (rows, PACK_COLS), key
    return loss_part, grad_x, riders.reduced, rep_grads, small_grads


def _reduce_small_gradients(rep_grads, small_grads):
    misc_local = _pack([rep_grads[n] for n in REP_NAMES] + [small_grads[n] for n in SMALL_NAMES])
    (misc_all,) = _comm_only(_gather_comm([misc_local]), "gather_small_grads")
    return _sum_leading(misc_all, "small_grads_sum")


def _update(inp, loss_part, grad_x, big_reduced, misc_sum):
    dev = _device_index()
    piece = lambda key, rows=None: big_reduced[key] if rows is None else big_reduced[key][:rows]
    both = lambda name, fn: jnp.stack([fn(piece(name + '0')), fn(piece(name + '1'))])
    ident, transpose = (lambda a: a), (lambda a: a.T)
    grads = {'w_in_ab': piece('w_in_t', W_IN_PIECE).T[None], 'w_glu_b': piece('w_glu').reshape(inp['w_glu_b'].shape),
             'w_out_ab': piece('w_out')[None], 'pool_w': piece('pool_w').reshape(inp['pool_w'].shape),
             'xa_wq': both('wq', ident), 'xa_wkv': both('wkv_t', transpose), 'xa_wo': both('wo', ident),
             'ffn_w_up': both('up_t', transpose), 'ffn_w_down': both('down', ident)}
    misc = _unpack(misc_sum, [inp[n].shape for n in REP_NAMES] + [SMALL_SHARDED[n][0] for n in SMALL_NAMES])
    for n, g in zip(REP_NAMES, misc):
        grads[n] = g
    for n, g in zip(SMALL_NAMES, misc[len(REP_NAMES):]):
        grads[n] = lax.dynamic_index_in_dim(_split_shards(g, SMALL_SHARDED[n][1]), dev, 0, keepdims=False
                                            ).reshape(inp[n].shape)
    upd = {}
    for n in BIG_NAMES:
        upd[n] = _adamw(inp[n], grads[n], inp['m_' + n], inp['v_' + n], "adamw_" + n)
    tiny_names = REP_NAMES + SMALL_NAMES
    rep_total = sum(int(np.prod(inp[n].shape)) for n in REP_NAMES)
    packs = [_pack([inp[prefix + n] for n in tiny_names]) for prefix in ('', 'm_', 'v_')]
    g_pack = _pack([misc_sum.reshape(-1)[:rep_total]] + [grads[n] for n in SMALL_NAMES])
    tiny_out = [_unpack(o, [inp[n].shape for n in tiny_names])
                for o in _adamw(packs[0], g_pack, packs[1], packs[2], "adamw_small")]
    for i, n in enumerate(tiny_names):
        upd[n] = tuple(o[i] for o in tiny_out)

    loss = lax.psum(loss_part[0, 0], ("x", "y", "c"))
    outs = [loss, grad_x[None]]
    outs += [grads[n] for n in WEIGHT_NAMES]
    for i in range(3):
        outs += [upd[n][i] for n in WEIGHT_NAMES]
    return tuple(outs)


def _step(inp):
    loss_part, grad_x, big_reduced, rep_grads, small_grads = _local_step(inp)
    misc_sum = _reduce_small_gradients(rep_grads, small_grads)
    return _update(inp, loss_part, grad_x, big_reduced, misc_sum)


INPUT_NAMES = (['x', 'mem'] + WEIGHT_NAMES + ['loss_target'] + ['m_' + n for n in WEIGHT_NAMES]
               + ['v_' + n for n in WEIGHT_NAMES])


def kernel(x, mem, norm_mix_g, norm_xa_g, norm_ffn_g, norm_mem_g, norm_final_g, w_in_ab, conv_qkv_a, a_log_a, dt_bias_a, onorm_g_a, ssm_lambda_re, ssm_lambda_im, ssm_b_re, ssm_b_im, ssm_c_re, ssm_c_im, ssm_d, ssm_log_dt, w_glu_b, b_glu_b, w_out_ab, pool_w, pool_scale, xa_wq, xa_wkv, xa_wo, ffn_w_up, ffn_conv, ffn_w_down, loss_target, m_norm_mix_g, m_norm_xa_g, m_norm_ffn_g, m_norm_mem_g, m_norm_final_g, m_w_in_ab, m_conv_qkv_a, m_a_log_a, m_dt_bias_a, m_onorm_g_a, m_ssm_lambda_re, m_ssm_lambda_im, m_ssm_b_re, m_ssm_b_im, m_ssm_c_re, m_ssm_c_im, m_ssm_d, m_ssm_log_dt, m_w_glu_b, m_b_glu_b, m_w_out_ab, m_pool_w, m_pool_scale, m_xa_wq, m_xa_wkv, m_xa_wo, m_ffn_w_up, m_ffn_conv, m_ffn_w_down, v_norm_mix_g, v_norm_xa_g, v_norm_ffn_g, v_norm_mem_g, v_norm_final_g, v_w_in_ab, v_conv_qkv_a, v_a_log_a, v_dt_bias_a, v_onorm_g_a, v_ssm_lambda_re, v_ssm_lambda_im, v_ssm_b_re, v_ssm_b_im, v_ssm_c_re, v_ssm_c_im, v_ssm_d, v_ssm_log_dt, v_w_glu_b, v_b_glu_b, v_w_out_ab, v_pool_w, v_pool_scale, v_xa_wq, v_xa_wkv, v_xa_wo, v_ffn_w_up, v_ffn_conv, v_ffn_w_down):
    args = (x, mem, norm_mix_g, norm_xa_g, norm_ffn_g, norm_mem_g, norm_final_g, w_in_ab, conv_qkv_a, a_log_a, dt_bias_a, onorm_g_a, ssm_lambda_re, ssm_lambda_im, ssm_b_re, ssm_b_im, ssm_c_re, ssm_c_im, ssm_d, ssm_log_dt, w_glu_b, b_glu_b, w_out_ab, pool_w, pool_scale, xa_wq, xa_wkv, xa_wo, ffn_w_up, ffn_conv, ffn_w_down, loss_target, m_norm_mix_g, m_norm_xa_g, m_norm_ffn_g, m_norm_mem_g, m_norm_final_g, m_w_in_ab, m_conv_qkv_a, m_a_log_a, m_dt_bias_a, m_onorm_g_a, m_ssm_lambda_re, m_ssm_lambda_im, m_ssm_b_re, m_ssm_b_im, m_ssm_c_re, m_ssm_c_im, m_ssm_d, m_ssm_log_dt, m_w_glu_b, m_b_glu_b, m_w_out_ab, m_pool_w, m_pool_scale, m_xa_wq, m_xa_wkv, m_xa_wo, m_ffn_w_up, m_ffn_conv, m_ffn_w_down, v_norm_mix_g, v_norm_xa_g, v_norm_ffn_g, v_norm_mem_g, v_norm_final_g, v_w_in_ab, v_conv_qkv_a, v_a_log_a, v_dt_bias_a, v_onorm_g_a, v_ssm_lambda_re, v_ssm_lambda_im, v_ssm_b_re, v_ssm_b_im, v_ssm_c_re, v_ssm_c_im, v_ssm_d, v_ssm_log_dt, v_w_glu_b, v_b_glu_b, v_w_out_ab, v_pool_w, v_pool_scale, v_xa_wq, v_xa_wkv, v_xa_wo, v_ffn_w_up, v_ffn_conv, v_ffn_w_down)
    return _step(dict(zip(INPUT_NAMES, args)))
```

```python
import functools
import math

import numpy as np
import jax
import jax.numpy as jnp
from jax import lax
from jax.experimental import pallas as pl
from jax.experimental.pallas import tpu as pltpu

F32, BF16 = jnp.float32, jnp.bfloat16
HIGH, HIGHEST = lax.Precision.HIGH, lax.Precision.HIGHEST
MESH = pl.DeviceIdType.MESH

N_DEV = 8
SEQ, D_MODEL, MEM_LEN = 2048, 1024, 256
WIDTH_A, N_HEADS_A, HEAD_A, CONV_A = 512, 4, 128, 4
GDR_CHUNK = 128
SSM_WIDTH, SSM_GROUP, N_GROUPS, SSM_STATE = 512, 16, 32, 64
SSM_CH = N_GROUPS * SSM_STATE
SCAN_CB = 512
POOL_WINDOWS = (2, 4, 8, 16)
POOL_GROUP = 256
N_HEADS_X, HEAD_X = 4, 256
D_FF, CONV_FFN = 2816, 3
RMS_EPS = 1e-6
ADAM_LR, ADAM_B1, ADAM_B2, ADAM_EPS, ADAM_WD, ADAM_STEP = 0.001, 0.9, 0.999, 1e-08, 0.01, 10
LANE = 128
PACK_COLS = 1024
VMEM_LIMIT_BYTES = 56 * 1024 * 1024


def _params(sem=None):
    return pltpu.CompilerParams(dimension_semantics=sem, vmem_limit_bytes=VMEM_LIMIT_BYTES)


class Comm:
    def __init__(self, inputs, out_shapes, sems, start, end, mid=None):
        self.inputs, self.out_shapes, self.sems = list(inputs), list(out_shapes), list(sems)
        self.start, self.mid, self.end = start, mid, end


def _merge_comms(comms):
    comms = [c for c in comms if c is not None]
    if not comms:
        return None, []
    bounds, ni, no, ns = [], 0, 0, 0
    for c in comms:
        bounds.append((ni, no, ns))
        ni, no, ns = ni + len(c.inputs), no + len(c.out_shapes), ns + len(c.sems)

    def phase(which):
        def run(ins, outs, sems):
            for c, (i0, o0, s0) in zip(comms, bounds):
                fn = getattr(c, which)
                if fn is not None:
                    fn(ins[i0:i0 + len(c.inputs)], outs[o0:o0 + len(c.out_shapes)], sems[s0:s0 + len(c.sems)])
        return run

    merged = Comm([a for c in comms for a in c.inputs], [s for c in comms for s in c.out_shapes],
                  [s for c in comms for s in c.sems], phase("start"), phase("end"), phase("mid"))
    return merged, [(o0, o0 + len(c.out_shapes)) for c, (_, o0, _) in zip(comms, bounds)]


def _call(body, *, name, grid, in_specs, out_specs, out_shape, args, scratch_shapes=(), sem=None, comm=None):
    single = not isinstance(out_shape, (list, tuple))
    out_specs_l = [out_specs] if single else list(out_specs)
    out_shape_l = [out_shape] if single else list(out_shape)
    scratch_shapes = list(scratch_shapes)
    merged, spans = _merge_comms(comm if isinstance(comm, (list, tuple)) else [comm])
    if merged is None:
        outs = pl.pallas_call(body, name=name, grid=grid, in_specs=list(in_specs), out_specs=out_specs_l,
                              out_shape=out_shape_l, scratch_shapes=scratch_shapes, compiler_params=_params(sem))(*args)
        outs = outs[0] if single else outs
        return outs if comm is None else (outs, [])
    n_in, n_out, n_scr = len(in_specs), len(out_specs_l), len(scratch_shapes)
    ci, co = len(merged.inputs), len(merged.out_shapes)
    total = int(np.prod(grid))

    def wrapped(*refs):
        ins, cins = refs[:n_in], refs[n_in:n_in + ci]
        outs, couts = refs[n_in + ci:n_in + ci + n_out], refs[n_in + ci + n_out:n_in + ci + n_out + co]
        scr, csems = refs[n_in + ci + n_out + co:n_in + ci + n_out + co + n_scr], refs[n_in + ci + n_out + co + n_scr:]
        lin = pl.program_id(0)
        for d in range(1, len(grid)):
            lin = lin * grid[d] + pl.program_id(d)
        pl.when(lin == 0)(lambda: merged.start(cins, couts, csems))
        body(*ins, *outs, *scr)
        def finish():
            merged.mid(cins, couts, csems)
            merged.end(cins, couts, csems)

        pl.when(lin == total - 1)(finish)

    any_spec = pl.BlockSpec(memory_space=pl.ANY)
    res = pl.pallas_call(
        wrapped, name=name, grid=grid, in_specs=list(in_specs) + [any_spec] * ci,
        out_specs=out_specs_l + [any_spec] * co, out_shape=out_shape_l + merged.out_shapes,
        scratch_shapes=scratch_shapes + merged.sems,
        compiler_params=_params(("arbitrary",) * len(grid)))(*args, *merged.inputs)
    outs, couts = res[:n_out], res[n_out:]
    return (outs[0] if single else list(outs)), [list(couts[a:b]) for a, b in spans]


def _comm_only(comm, name):
    def body():
        pass

    _, couts = _call(body, name=name, grid=(1,), in_specs=[], out_specs=[], out_shape=[], args=[], comm=comm)
    return couts[0]


def _tile(dim, pref):
    best = None
    for t in range(LANE, min(dim, pref) + 1, LANE):
        if dim % t == 0:
            best = t
    return best if best is not None else dim


MM_VMEM_BUDGET = 40 * 1024 * 1024


def _mm_tiles(m, n, k, a_bytes, b_bytes, o_bytes, r_bytes):
    for tk in (k, _tile(k, 2048), _tile(k, 1024), _tile(k, 512)):
        for tm, tn in ((1024, 1536), (1024, 1024), (1024, 512), (512, 512), (256, 512), (256, 256)):
            tm, tn = _tile(m, tm), _tile(n, tn)
            acc = 0 if tk == k else tm * tn * 4
            need = 2 * (tm * tk * a_bytes + tk * tn * b_bytes + tm * tn * (o_bytes + r_bytes)) + acc
            if need <= MM_VMEM_BUDGET:
                return tm, tn, tk
    raise ValueError("no matmul tiling fits VMEM")


def _mm(a, b, mode, name, out_dtype=F32, res=None, comm=None):
    if mode == "nn":
        (m, k), n = a.shape, b.shape[1]
    elif mode == "nt":
        (m, k), n = a.shape, b.shape[0]
    else:
        (k, m), n = a.shape, b.shape[1]
    tm, tn, tk = _mm_tiles(m, n, k, a.dtype.itemsize, b.dtype.itemsize, jnp.dtype(out_dtype).itemsize,
                           0 if res is None else res.dtype.itemsize)
    nk = k // tk
    dims = {"nn": ((1,), (0,)), "nt": ((1,), (1,)), "tn": ((0,), (0,))}[mode]

    def body(*refs):
        if res is None:
            a_ref, b_ref, o_ref = refs[:3]
            r_ref = None
        else:
            a_ref, b_ref, r_ref, o_ref = refs[:4]
        part = lax.dot_general(a_ref[...].astype(BF16), b_ref[...].astype(BF16), (dims, ((), ())),
                               preferred_element_type=F32)

        def finish(out):
            if r_ref is not None:
                out = out + r_ref[...].astype(F32)
            o_ref[...] = out.astype(out_dtype)

        if nk == 1:
            finish(part)
            return
        acc = refs[-1]
        kk = pl.program_id(2)

        @pl.when(kk == 0)
        def _():
            acc[...] = part

        @pl.when(kk > 0)
        def _():
            acc[...] += part

        @pl.when(kk == nk - 1)
        def _():
            finish(acc[...])

    a_spec = (pl.BlockSpec((tk, tm), lambda i, j, q: (q, i)) if mode == "tn"
              else pl.BlockSpec((tm, tk), lambda i, j, q: (i, q)))
    b_spec = (pl.BlockSpec((tn, tk), lambda i, j, q: (j, q)) if mode == "nt"
              else pl.BlockSpec((tk, tn), lambda i, j, q: (q, j)))
    o_spec = pl.BlockSpec((tm, tn), lambda i, j, q: (i, j))
    in_specs, args = [a_spec, b_spec], [a, b]
    if res is not None:
        in_specs.append(o_spec)
        args.append(res)
    return _call(body, name=name, grid=(m // tm, n // tn, nk), in_specs=in_specs, out_specs=o_spec,
                 out_shape=jax.ShapeDtypeStruct((m, n), out_dtype),
                 scratch_shapes=[] if nk == 1 else [pltpu.VMEM((tm, tn), F32)],
                 sem=("parallel", "parallel", "arbitrary"), args=args, comm=comm)


def _mm_bd(a, b, mode, name, out_dtype=F32, res=None, comm=None, tm=1024):
    if mode == "tn":
        k = a.shape[0]
        nb = min(a.shape[1], b.shape[1]) // LANE
        ma, n = a.shape[1] // nb, b.shape[1] // nb

        def body(a_ref, b_ref, o_ref):
            o_ref[0] = lax.dot_general(a_ref[...].astype(BF16), b_ref[...].astype(BF16), (((0,), (0,)), ((), ())),
                                       preferred_element_type=F32).astype(out_dtype)

        return _call(body, name=name, grid=(nb,),
                     in_specs=[pl.BlockSpec((k, ma), lambda j: (0, j)), pl.BlockSpec((k, n), lambda j: (0, j))],
                     out_specs=pl.BlockSpec((1, ma, n), lambda j: (j, 0, 0)),
                     out_shape=jax.ShapeDtypeStruct((nb, ma, n), out_dtype), sem=("parallel",), args=(a, b), comm=comm)
    m = a.shape[0]
    nb = b.shape[0]
    ka = a.shape[1] // nb
    n = b.shape[2] if mode == "nn" else b.shape[1]
    tm = _tile(m, tm)
    dims = ((1,), (0,)) if mode == "nn" else ((1,), (1,))

    def body(*refs):
        if res is None:
            a_ref, b_ref, o_ref = refs
            r_ref = None
        else:
            a_ref, b_ref, r_ref, o_ref = refs
        out = lax.dot_general(a_ref[...].astype(BF16), b_ref[0].astype(BF16), (dims, ((), ())),
                              preferred_element_type=F32)
        if r_ref is not None:
            out = out + r_ref[...].astype(F32)
        o_ref[...] = out.astype(out_dtype)

    o_spec = pl.BlockSpec((tm, n), lambda i, j: (i, j))
    in_specs = [pl.BlockSpec((tm, ka), lambda i, j: (i, j)), pl.BlockSpec((1,) + b.shape[1:], lambda i, j: (j, 0, 0))]
    args = [a, b]
    if res is not None:
        in_specs.append(o_spec)
        args.append(res)
    return _call(body, name=name, grid=(m // tm, nb), in_specs=in_specs, out_specs=o_spec,
                 out_shape=jax.ShapeDtypeStruct((m, nb * n), out_dtype), sem=("parallel", "parallel"),
                 args=args, comm=comm)


def _rms_fwd(x, g, out_dtype, name, tr=256):
    rows, d = x.shape

    def body(x_ref, g_ref, o_ref):
        xv = x_ref[...]
        r = lax.rsqrt(jnp.mean(xv * xv, axis=-1, keepdims=True) + RMS_EPS)
        o_ref[...] = (xv * r * g_ref[...]).astype(out_dtype)

    return pl.pallas_call(
        body, name=name, grid=(rows // tr,),
        in_specs=[pl.BlockSpec((tr, d), lambda i: (i, 0)), pl.BlockSpec((1, d), lambda i: (0, 0))],
        out_specs=pl.BlockSpec((tr, d), lambda i: (i, 0)), out_shape=jax.ShapeDtypeStruct((rows, d), out_dtype),
        compiler_params=_params(("parallel",)))(x, g.reshape(1, d))


def _rms_bwd(x, g, dy, dres, name, tr=256, comm=None):
    rows, d = x.shape

    def body(*refs):
        if dres is None:
            x_ref, g_ref, dy_ref, dx_ref, dg_ref = refs
            r_ref = None
        else:
            x_ref, g_ref, dy_ref, r_ref, dx_ref, dg_ref = refs

        @pl.when(pl.program_id(0) == 0)
        def _():
            dg_ref[...] = jnp.zeros_like(dg_ref)

        xv, dyv = x_ref[...], dy_ref[...].astype(F32)
        r = lax.rsqrt(jnp.mean(xv * xv, axis=-1, keepdims=True) + RMS_EPS)
        xh = xv * r
        dyg = dyv * g_ref[...]
        dx = r * (dyg - xh * jnp.mean(dyg * xh, axis=-1, keepdims=True))
        if r_ref is not None:
            dx = dx + r_ref[...]
        dx_ref[...] = dx
        dg_ref[...] += jnp.sum(dyv * xh, axis=0, keepdims=True)

    blk = pl.BlockSpec((tr, d), lambda i: (i, 0))
    vec = pl.BlockSpec((1, d), lambda i: (0, 0))
    in_specs, args = [blk, vec, blk], [x, g.reshape(1, d), dy]
    if dres is not None:
        in_specs.append(blk)
        args.append(dres)
    return _call(
        body, name=name, grid=(rows // tr,), in_specs=in_specs, out_specs=[blk, vec],
        out_shape=[jax.ShapeDtypeStruct((rows, d), F32), jax.ShapeDtypeStruct((1, d), F32)],
        sem=("arbitrary",), args=args, comm=comm)


def _loss_head(x, g, target, name, tr=256):
    rows, d = x.shape

    def body(x_ref, g_ref, t_ref, loss_ref, dx_ref, dg_ref):
        @pl.when(pl.program_id(0) == 0)
        def _():
            dg_ref[...] = jnp.zeros_like(dg_ref)
            loss_ref[...] = jnp.zeros_like(loss_ref)

        xv = x_ref[...]
        r = lax.rsqrt(jnp.mean(xv * xv, axis=-1, keepdims=True) + RMS_EPS)
        xh = xv * r
        err = xh * g_ref[...] - t_ref[...]
        loss_ref[...] += 0.5 * jnp.sum(jnp.mean(err * err, axis=-1, keepdims=True), keepdims=True)
        dyv = err * (1.0 / d)
        dyg = dyv * g_ref[...]
        dx_ref[...] = r * (dyg - xh * jnp.mean(dyg * xh, axis=-1, keepdims=True))
        dg_ref[...] += jnp.sum(dyv * xh, axis=0, keepdims=True)

    blk = pl.BlockSpec((tr, d), lambda i: (i, 0))
    vec = pl.BlockSpec((1, d), lambda i: (0, 0))
    return pl.pallas_call(
        body, name=name, grid=(rows // tr,), in_specs=[blk, vec, blk],
        out_specs=[pl.BlockSpec((1, 1), lambda i: (0, 0)), blk, vec],
        out_shape=[jax.ShapeDtypeStruct((1, 1), F32), jax.ShapeDtypeStruct((rows, d), F32),
                   jax.ShapeDtypeStruct((1, d), F32)],
        compiler_params=_params(("arbitrary",)))(x, g.reshape(1, d), target)


def _shift_down(x, s):
    rows = lax.broadcasted_iota(jnp.int32, x.shape, 0)
    return jnp.where(rows >= s, pltpu.roll(x, s, 0), 0.0)


def _shift_up(x, s):
    n = x.shape[0]
    rows = lax.broadcasted_iota(jnp.int32, x.shape, 0)
    return jnp.where(rows < n - s, pltpu.roll(x, n - s, 0), 0.0)


def _sigmoid(x):
    return 1.0 / (1.0 + jnp.exp(-x))


def _silu_and_grad(x):
    s = _sigmoid(x)
    return x * s, s * (1.0 + x * (1.0 - s))


_GELU_C0, _GELU_C1 = math.sqrt(2.0 / math.pi), 0.044715


def _gelu_and_grad(x):
    th = jnp.tanh(_GELU_C0 * (x + _GELU_C1 * x * x * x))
    y = 0.5 * x * (1.0 + th)
    dy = 0.5 * (1.0 + th) + 0.5 * x * (1.0 - th * th) * _GELU_C0 * (1.0 + 3.0 * _GELU_C1 * x * x)
    return y, dy


def _ffn_act_fwd(h, w, name, tc=256, comm=None):
    t = h.shape[0]
    nb = D_FF // tc

    def body(hg_ref, hv_ref, wg_ref, wv_ref, a_ref):
        def conv(x, wr):
            return wr[2:3, :] * x + wr[1:2, :] * _shift_down(x, 1) + wr[0:1, :] * _shift_down(x, 2)

        cg = conv(hg_ref[...], wg_ref[...])
        cv = conv(hv_ref[...], wv_ref[...])
        a_ref[...] = (cg * _sigmoid(cg) * cv).astype(BF16)

    return _call(
        body, name=name, grid=(nb,),
        in_specs=[pl.BlockSpec((t, tc), lambda j: (0, j)), pl.BlockSpec((t, tc), lambda j: (0, j + nb)),
                  pl.BlockSpec((CONV_FFN, tc), lambda j: (0, j)), pl.BlockSpec((CONV_FFN, tc), lambda j: (0, j + nb))],
        out_specs=pl.BlockSpec((t, tc), lambda j: (0, j)), out_shape=jax.ShapeDtypeStruct((t, D_FF), BF16),
        sem=("parallel",), args=(h, h, w, w), comm=comm)


def _ffn_act_bwd(h, w, da, name, tc=256, comm=None):
    t = h.shape[0]
    nb = D_FF // tc

    def body(hg_ref, hv_ref, wg_ref, wv_ref, da_ref, dhg_ref, dhv_ref, dwg_ref, dwv_ref):
        hg, hv, wg, wv = hg_ref[...], hv_ref[...], wg_ref[...], wv_ref[...]
        hg1, hg2, hv1, hv2 = _shift_down(hg, 1), _shift_down(hg, 2), _shift_down(hv, 1), _shift_down(hv, 2)
        cg = wg[2:3, :] * hg + wg[1:2, :] * hg1 + wg[0:1, :] * hg2
        cv = wv[2:3, :] * hv + wv[1:2, :] * hv1 + wv[0:1, :] * hv2
        sg, dsg = _silu_and_grad(cg)
        dav = da_ref[...].astype(F32)
        dcv = dav * sg
        dcg = dav * cv * dsg

        def conv_t(dc, wr):
            return wr[2:3, :] * dc + wr[1:2, :] * _shift_up(dc, 1) + wr[0:1, :] * _shift_up(dc, 2)

        dhg_ref[...] = conv_t(dcg, wg).astype(BF16)
        dhv_ref[...] = conv_t(dcv, wv).astype(BF16)
        dwg_ref[0:1, :] = jnp.sum(dcg * hg2, axis=0, keepdims=True)
        dwg_ref[1:2, :] = jnp.sum(dcg * hg1, axis=0, keepdims=True)
        dwg_ref[2:3, :] = jnp.sum(dcg * hg, axis=0, keepdims=True)
        dwv_ref[0:1, :] = jnp.sum(dcv * hv2, axis=0, keepdims=True)
        dwv_ref[1:2, :] = jnp.sum(dcv * hv1, axis=0, keepdims=True)
        dwv_ref[2:3, :] = jnp.sum(dcv * hv, axis=0, keepdims=True)

    big = lambda off: pl.BlockSpec((t, tc), lambda j: (0, j + off))
    small = lambda off: pl.BlockSpec((CONV_FFN, tc), lambda j: (0, j + off))
    res = _call(
        body, name=name, grid=(nb,),
        in_specs=[big(0), big(nb), small(0), small(nb), big(0)],
        out_specs=[big(0), big(0), small(0), small(0)],
        out_shape=[jax.ShapeDtypeStruct((t, D_FF), BF16), jax.ShapeDtypeStruct((t, D_FF), BF16),
                   jax.ShapeDtypeStruct((CONV_FFN, D_FF), F32), jax.ShapeDtypeStruct((CONV_FFN, D_FF), F32)],
        sem=("parallel",), args=(h, h, w, w, da), comm=comm)
    (dhg, dhv, dwg, dwv), couts = res if comm is not None else (res, None)
    out = (jnp.concatenate([dhg, dhv], axis=1), jnp.concatenate([dwg, dwv], axis=1))
    return out if comm is None else (out, couts)


def _attn_probs(q, k):
    s = lax.dot_general(q.astype(BF16), k.astype(BF16), (((1,), (1,)), ((), ())),
                        preferred_element_type=F32) * (HEAD_X ** -0.5)
    s = s - jnp.max(s, axis=-1, keepdims=True)
    p = jnp.exp(s)
    return p / jnp.sum(p, axis=-1, keepdims=True)


def _attn_fwd(q, kv, name, tq=512, comm=None):
    t = q.shape[0]

    def body(q_ref, k_ref, v_ref, o_ref):
        p = _attn_probs(q_ref[...], k_ref[...])
        o_ref[...] = jnp.dot(p.astype(BF16), v_ref[...].astype(BF16), preferred_element_type=F32).astype(BF16)

    return _call(
        body, name=name, grid=(N_HEADS_X, t // tq),
        in_specs=[pl.BlockSpec((tq, HEAD_X), lambda h, i: (i, h)),
                  pl.BlockSpec((MEM_LEN, HEAD_X), lambda h, i: (0, h)),
                  pl.BlockSpec((MEM_LEN, HEAD_X), lambda h, i: (0, h + N_HEADS_X))],
        out_specs=pl.BlockSpec((tq, HEAD_X), lambda h, i: (i, h)),
        out_shape=jax.ShapeDtypeStruct((t, N_HEADS_X * HEAD_X), BF16),
        sem=("parallel", "parallel"), args=(q, kv, kv), comm=comm)


def _attn_bwd(q, kv, do, name, tq=512):
    t = q.shape[0]

    def body(q_ref, k_ref, v_ref, do_ref, dq_ref, dk_ref, dv_ref):
        @pl.when(pl.program_id(1) == 0)
        def _():
            dk_ref[...] = jnp.zeros_like(dk_ref)
            dv_ref[...] = jnp.zeros_like(dv_ref)

        qb, kb, vb, dob = (r[...].astype(BF16) for r in (q_ref, k_ref, v_ref, do_ref))
        p = _attn_probs(qb, kb)
        dp = lax.dot_general(dob, vb, (((1,), (1,)), ((), ())), preferred_element_type=F32)
        ds = p * (dp - jnp.sum(dp * p, axis=-1, keepdims=True)) * (HEAD_X ** -0.5)
        dsb = ds.astype(BF16)
        dq_ref[...] = jnp.dot(dsb, kb, preferred_element_type=F32).astype(BF16)
        dk_ref[...] += lax.dot_general(dsb, qb, (((0,), (0,)), ((), ())), preferred_element_type=F32)
        dv_ref[...] += lax.dot_general(p.astype(BF16), dob, (((0,), (0,)), ((), ())), preferred_element_type=F32)

    qs = pl.BlockSpec((tq, HEAD_X), lambda h, i: (i, h))
    ms = pl.BlockSpec((MEM_LEN, HEAD_X), lambda h, i: (0, h))
    return pl.pallas_call(
        body, name=name, grid=(N_HEADS_X, t // tq),
        in_specs=[qs, ms, pl.BlockSpec((MEM_LEN, HEAD_X), lambda h, i: (0, h + N_HEADS_X)), qs],
        out_specs=[qs, ms, ms],
        out_shape=[jax.ShapeDtypeStruct((t, D_MODEL), BF16), jax.ShapeDtypeStruct((MEM_LEN, D_MODEL), F32),
                   jax.ShapeDtypeStruct((MEM_LEN, D_MODEL), F32)],
        compiler_params=_params(("parallel", "arbitrary")))(q, kv, kv, do)


def _pool_counts(t, win):
    pos = lax.broadcasted_iota(jnp.int32, (t, 1), 0).astype(F32) + 1.0
    return 1.0 / jnp.minimum(pos, float(win))


def _pool_delta(xv, win):
    s, step = xv, 1
    while step < win:
        s = s + _shift_down(s, step)
        step *= 2
    return s * _pool_counts(xv.shape[0], win) - xv


def _pool_delta_t(dv, win):
    s, step = dv * _pool_counts(dv.shape[0], win), 1
    while step < win:
        s = s + _shift_up(s, step)
        step *= 2
    return s - dv


def _pool_fwd(xn, w, scale, res, name):
    t = xn.shape[0]

    def make_branch(win, xn_ref, w_ref, s_ref, r_ref, o_ref):
        def branch():
            dl = _pool_delta(xn_ref[...], win)
            y = jnp.dot(dl.astype(BF16), w_ref[0], preferred_element_type=F32)
            o_ref[...] = r_ref[...] + y * s_ref[...]
        return branch

    def body(xn_ref, w_ref, s_ref, r_ref, o_ref):
        for gi, win in enumerate(POOL_WINDOWS):
            pl.when(pl.program_id(0) == gi)(make_branch(win, xn_ref, w_ref, s_ref, r_ref, o_ref))

    blk = pl.BlockSpec((t, POOL_GROUP), lambda g: (0, g))
    return pl.pallas_call(
        body, name=name, grid=(len(POOL_WINDOWS),),
        in_specs=[blk, pl.BlockSpec((1, POOL_GROUP, POOL_GROUP), lambda g: (g, 0, 0)),
                  pl.BlockSpec((1, POOL_GROUP), lambda g: (0, g)), blk],
        out_specs=blk, out_shape=jax.ShapeDtypeStruct((t, D_MODEL), F32),
        compiler_params=_params(("parallel",)))(xn, w, scale, res)


def _pool_bwd(xn, w, scale, dmix, name):
    t = xn.shape[0]

    def make_branch(win, xn_ref, w_ref, s_ref, d_ref, dxn_ref, dw_ref, ds_ref):
        def branch():
            dl = _pool_delta(xn_ref[...], win).astype(BF16)
            wv = w_ref[0]
            dm = d_ref[...]
            y = jnp.dot(dl, wv, preferred_element_type=F32)
            ds_ref[...] = jnp.sum(dm * y, axis=0, keepdims=True)
            dy = (dm * s_ref[...]).astype(BF16)
            dw_ref[0] = lax.dot_general(dl, dy, (((0,), (0,)), ((), ())), preferred_element_type=F32)
            ddl = lax.dot_general(dy, wv, (((1,), (1,)), ((), ())), preferred_element_type=F32)
            dxn_ref[...] = _pool_delta_t(ddl, win)
        return branch

    def body(*refs):
        for gi, win in enumerate(POOL_WINDOWS):
            pl.when(pl.program_id(0) == gi)(make_branch(win, *refs))

    blk = pl.BlockSpec((t, POOL_GROUP), lambda g: (0, g))
    wspec = pl.BlockSpec((1, POOL_GROUP, POOL_GROUP), lambda g: (g, 0, 0))
    vec = pl.BlockSpec((1, POOL_GROUP), lambda g: (0, g))
    return pl.pallas_call(
        body, name=name, grid=(len(POOL_WINDOWS),), in_specs=[blk, wspec, vec, blk], out_specs=[blk, wspec, vec],
        out_shape=[jax.ShapeDtypeStruct((t, D_MODEL), F32),
                   jax.ShapeDtypeStruct((len(POOL_WINDOWS), POOL_GROUP, POOL_GROUP), F32),
                   jax.ShapeDtypeStruct((1, D_MODEL), F32)],
        compiler_params=_params(("parallel",)))(xn, w, scale, dmix)


def _qkv_conv(h, wr):
    return (wr[3:4, :] * h + wr[2:3, :] * _shift_down(h, 1) + wr[1:2, :] * _shift_down(h, 2)
            + wr[0:1, :] * _shift_down(h, 3))


def _qkv_pre_fwd(h, w, col0, ncols, normalize, scale, name):
    t = h.shape[0]

    def body(h_ref, w_ref, o_ref):
        c = _qkv_conv(h_ref[...], w_ref[...])
        s = c * _sigmoid(c)
        if normalize:
            s = s * lax.rsqrt(jnp.sum(s * s, axis=-1, keepdims=True) + 1e-6) * scale
        o_ref[...] = s

    return pl.pallas_call(
        body, name=name, grid=(ncols,),
        in_specs=[pl.BlockSpec((t, HEAD_A), lambda j: (0, j + col0)), pl.BlockSpec((CONV_A, HEAD_A), lambda j: (0, j + col0))],
        out_specs=pl.BlockSpec((t, HEAD_A), lambda j: (0, j)), out_shape=jax.ShapeDtypeStruct((t, ncols * HEAD_A), F32),
        compiler_params=_params(("parallel",)))(h, w)


def _qkv_pre_bwd(h, w, dy, col0, ncols, normalize, scale, name):
    t = h.shape[0]

    def body(h_ref, w_ref, dy_ref, dh_ref, dw_ref):
        hv, wr, dyv = h_ref[...], w_ref[...], dy_ref[...]
        h1, h2, h3 = _shift_down(hv, 1), _shift_down(hv, 2), _shift_down(hv, 3)
        c = wr[3:4, :] * hv + wr[2:3, :] * h1 + wr[1:2, :] * h2 + wr[0:1, :] * h3
        s, dsilu = _silu_and_grad(c)
        if normalize:
            r = lax.rsqrt(jnp.sum(s * s, axis=-1, keepdims=True) + 1e-6)
            y = s * r
            dyv = dyv * scale
            ds = r * (dyv - y * jnp.sum(dyv * y, axis=-1, keepdims=True))
        else:
            ds = dyv
        dc = ds * dsilu
        dh = (wr[3:4, :] * dc + wr[2:3, :] * _shift_up(dc, 1) + wr[1:2, :] * _shift_up(dc, 2)
              + wr[0:1, :] * _shift_up(dc, 3))
        dh_ref[...] = dh.astype(BF16)
        dw_ref[0:1, :] = jnp.sum(dc * h3, axis=0, keepdims=True)
        dw_ref[1:2, :] = jnp.sum(dc * h2, axis=0, keepdims=True)
        dw_ref[2:3, :] = jnp.sum(dc * h1, axis=0, keepdims=True)
        dw_ref[3:4, :] = jnp.sum(dc * hv, axis=0, keepdims=True)

    return pl.pallas_call(
        body, name=name, grid=(ncols,),
        in_specs=[pl.BlockSpec((t, HEAD_A), lambda j: (0, j + col0)), pl.BlockSpec((CONV_A, HEAD_A), lambda j: (0, j + col0)),
                  pl.BlockSpec((t, HEAD_A), lambda j: (0, j))],
        out_specs=[pl.BlockSpec((t, HEAD_A), lambda j: (0, j)), pl.BlockSpec((CONV_A, HEAD_A), lambda j: (0, j))],
        out_shape=[jax.ShapeDtypeStruct((t, ncols * HEAD_A), BF16), jax.ShapeDtypeStruct((CONV_A, ncols * HEAD_A), F32)],
        compiler_params=_params(("parallel",)))(h, w, dy)


def _softplus(x):
    return jnp.maximum(x, 0.0) + jnp.log1p(jnp.exp(-jnp.abs(x)))


def _gates_fwd(ba, arow, brow, name):
    t = ba.shape[0]

    def body(x_ref, a_ref, b_ref, o_ref):
        xv = x_ref[...]
        lane = lax.broadcasted_iota(jnp.int32, xv.shape, 1)
        beta = _sigmoid(xv)
        g = -jnp.exp(a_ref[...]) * _softplus(xv + b_ref[...])
        o_ref[...] = jnp.where(lane < N_HEADS_A, beta, jnp.where(lane < 2 * N_HEADS_A, g, 0.0))

    return pl.pallas_call(body, name=name, out_shape=jax.ShapeDtypeStruct((t, LANE), F32),
                          compiler_params=_params())(ba, arow, brow)


def _gates_bwd(ba, arow, brow, dgb, name):
    t = ba.shape[0]

    def body(x_ref, a_ref, b_ref, d_ref, dx_ref, da_ref, db_ref):
        xv = x_ref[...]
        dv = d_ref[0] + d_ref[1] + d_ref[2] + d_ref[3]
        lane = lax.broadcasted_iota(jnp.int32, xv.shape, 1)
        beta = _sigmoid(xv)
        ea = jnp.exp(a_ref[...])
        z = xv + b_ref[...]
        dgv = jnp.where((lane >= N_HEADS_A) & (lane < 2 * N_HEADS_A), dv, 0.0) * (-ea)
        dz = dgv * _sigmoid(z)
        dx = jnp.where(lane < N_HEADS_A, dv * beta * (1.0 - beta), dz)
        dx_ref[...] = dx.astype(BF16)
        db_ref[...] = jnp.sum(dz, axis=0, keepdims=True)
        da_ref[...] = jnp.sum(dgv * _softplus(z), axis=0, keepdims=True)

    return pl.pallas_call(
        body, name=name,
        out_shape=[jax.ShapeDtypeStruct((t, LANE), BF16), jax.ShapeDtypeStruct((1, LANE), F32),
                   jax.ShapeDtypeStruct((1, LANE), F32)],
        compiler_params=_params())(ba, arow, brow, dgb)


def _dot(a, b, prec=None):
    if prec is None:
        return jnp.dot(a.astype(BF16), b.astype(BF16), preferred_element_type=F32)
    return jnp.dot(a, b, precision=prec, preferred_element_type=F32)


def _dot_nt(a, b, prec=None):
    if prec is None:
        a, b = a.astype(BF16), b.astype(BF16)
    return lax.dot_general(a, b, (((1,), (1,)), ((), ())), precision=prec, preferred_element_type=F32)


def _dot_tn(a, b, prec=None):
    if prec is None:
        a, b = a.astype(BF16), b.astype(BF16)
    return lax.dot_general(a, b, (((0,), (0,)), ((), ())), precision=prec, preferred_element_type=F32)


def _gdr_chunk_terms(k, beta, g):
    c = GDR_CHUNK
    row = lax.broadcasted_iota(jnp.int32, (c, c), 0)
    col = lax.broadcasted_iota(jnp.int32, (c, c), 1)
    causal, strict = row >= col, row > col
    gcum = _dot(causal.astype(F32), jnp.broadcast_to(g, (c, c)), HIGHEST)
    diff = gcum - gcum.T
    decay = jnp.where(causal, jnp.exp(jnp.where(causal, diff, 0.0)), 0.0)
    kb = k * beta
    kk = _dot_nt(kb, k)
    return row, col, causal, strict, gcum, decay, kb, kk


def _unit_lower_inverse(a):
    c = a.shape[0]
    eye = (lax.broadcasted_iota(jnp.int32, (c, c), 0) == lax.broadcasted_iota(jnp.int32, (c, c), 1)).astype(F32)
    p = -a
    inv = eye + p
    step = 1
    while 2 * step < c:
        p = _dot(p, p, HIGH)
        inv = inv + _dot(inv, p, HIGH)
        step *= 2
    return inv


def _head_gates(gates, head):
    lane = lax.broadcasted_iota(jnp.int32, gates.shape, 1)
    beta = jnp.sum(jnp.where(lane == head, gates, 0.0), axis=1, keepdims=True)
    g = jnp.sum(jnp.where(lane == head + N_HEADS_A, gates, 0.0), axis=1, keepdims=True)
    return beta, g


def _gdr_fwd(q, k, v, gates, name, comm=None):
    t = q.shape[0]
    c = GDR_CHUNK
    n = t // c

    def body(q_ref, k_ref, v_ref, gb_ref, o_ref, tm_ref, s_ref, state):
        @pl.when(pl.program_id(1) == 0)
        def _():
            state[...] = jnp.zeros_like(state)

        qv, kv, vv = q_ref[...], k_ref[...], v_ref[...]
        beta, g = _head_gates(gb_ref[...], pl.program_id(0))
        row, col, causal, strict, gcum, decay, kb, kk = _gdr_chunk_terms(kv, beta, g)
        tm = _unit_lower_inverse(jnp.where(strict, kk * decay, 0.0))
        e = jnp.exp(gcum)
        u = _dot(tm, vv * beta, HIGH)
        w = _dot(tm, kb * e, HIGH)
        p = jnp.where(causal, _dot_nt(qv, kv) * decay, 0.0)
        s = state[...]
        s_ref[0, 0] = s
        tm_ref[0, 0] = tm
        vn = u - _dot(w, s)
        o_ref[...] = _dot(qv * e, s) + _dot(p, vn)
        glast = gcum[c - 1:c, :]
        state[...] = s * jnp.exp(glast) + _dot_tn(kv * jnp.exp(glast - gcum), vn)

    blk = pl.BlockSpec((c, HEAD_A), lambda h, i: (i, h))
    mat = pl.BlockSpec((1, 1, c, c), lambda h, i: (h, i, 0, 0))
    return _call(
        body, name=name, grid=(N_HEADS_A, n),
        in_specs=[blk, blk, blk, pl.BlockSpec((c, LANE), lambda h, i: (i, 0))],
        out_specs=[blk, mat, mat],
        out_shape=[jax.ShapeDtypeStruct((t, WIDTH_A), F32), jax.ShapeDtypeStruct((N_HEADS_A, n, c, c), F32),
                   jax.ShapeDtypeStruct((N_HEADS_A, n, HEAD_A, HEAD_A), F32)],
        scratch_shapes=[pltpu.VMEM((HEAD_A, HEAD_A), F32)], sem=("parallel", "arbitrary"),
        args=(q, k, v, gates), comm=comm)


def _gdr_bwd(q, k, v, gates, tm_all, s_all, do, name, comm=None):
    t = q.shape[0]
    c = GDR_CHUNK
    n = t // c

    def body(q_ref, k_ref, v_ref, gb_ref, tm_ref, s_ref, do_ref, dq_ref, dk_ref, dv_ref, dgb_ref, dstate):
        @pl.when(pl.program_id(1) == 0)
        def _():
            dstate[...] = jnp.zeros_like(dstate)

        qv, kv, vv, dov = q_ref[...], k_ref[...], v_ref[...], do_ref[...]
        head = pl.program_id(0)
        beta, g = _head_gates(gb_ref[...], head)
        tm, s, dsp = tm_ref[0, 0], s_ref[0, 0], dstate[...]
        row, col, causal, strict, gcum, decay, kb, kk = _gdr_chunk_terms(kv, beta, g)
        e = jnp.exp(gcum)
        vb, kbe = vv * beta, kb * e
        u = _dot(tm, vb, HIGH)
        w = _dot(tm, kbe, HIGH)
        qk = _dot_nt(qv, kv)
        p = jnp.where(causal, qk * decay, 0.0)
        vn = u - _dot(w, s)
        glast = gcum[c - 1:c, :]
        el = jnp.exp(glast)
        f = jnp.exp(glast - gcum)
        kd = kv * f
        qe = qv * e

        dvn = _dot_tn(p, dov) + _dot(kd, dsp)
        dglast = el[:, 0:1] * jnp.sum(s * dsp, keepdims=True)
        dkd = _dot_nt(vn, dsp)
        dk = dkd * f
        df = jnp.sum(dkd * kv, axis=1, keepdims=True) * f[:, 0:1]
        dglast = dglast + jnp.sum(df, keepdims=True)
        dgc = -df
        dp = jnp.where(causal, _dot_nt(dov, vn), 0.0)
        dqe = _dot_nt(dov, s)
        dq = dqe * e
        de = jnp.sum(dqe * qv, axis=1, keepdims=True)
        dstate[...] = dsp * el + _dot_tn(qe, dov) - _dot_tn(w, dvn)
        dw = -_dot_nt(dvn, s)
        dvb = _dot_tn(tm, dvn, HIGH)
        dkbe = _dot_tn(tm, dw, HIGH)
        da = -jnp.where(strict, _dot_nt(dvb, u) + _dot_nt(dkbe, w), 0.0)
        dkk = da * decay
        dqk = dp * decay
        dd = da * kk + dp * qk
        dq = dq + _dot(dqk, kv)
        dk = dk + _dot_tn(dqk, qv)
        dkb = _dot(dkk, kv) + dkbe * e
        dk = dk + _dot_tn(dkk, kb)
        de = de + jnp.sum(dkbe * kb, axis=1, keepdims=True)
        dk = dk + dkb * beta
        dbeta = jnp.sum(dkb * kv, axis=1, keepdims=True) + jnp.sum(dvb * vv, axis=1, keepdims=True)
        m = dd * decay
        dgc = dgc + jnp.sum(m, axis=1, keepdims=True) - jnp.sum(m.T, axis=1, keepdims=True)
        dgc = dgc + de * e[:, 0:1]
        dgc = dgc + jnp.where(row[:, 0:1] == c - 1, dglast, 0.0)
        dg = _dot((row <= col).astype(F32), jnp.broadcast_to(dgc, (c, c)), HIGHEST)
        dq_ref[...] = dq
        dk_ref[...] = dk
        dv_ref[...] = dvb * beta
        lane = lax.broadcasted_iota(jnp.int32, (c, LANE), 1)
        dgb_ref[0] = jnp.where(lane == head, dbeta, jnp.where(lane == head + N_HEADS_A, dg, 0.0))

    blk = pl.BlockSpec((c, HEAD_A), lambda h, i: (n - 1 - i, h))
    mat = pl.BlockSpec((1, 1, c, c), lambda h, i: (h, n - 1 - i, 0, 0))
    return _call(
        body, name=name, grid=(N_HEADS_A, n),
        in_specs=[blk, blk, blk, pl.BlockSpec((c, LANE), lambda h, i: (n - 1 - i, 0)), mat, mat, blk],
        out_specs=[blk, blk, blk, pl.BlockSpec((1, c, LANE), lambda h, i: (h, n - 1 - i, 0))],
        out_shape=[jax.ShapeDtypeStruct((t, WIDTH_A), F32)] * 3 + [jax.ShapeDtypeStruct((N_HEADS_A, t, LANE), F32)],
        scratch_shapes=[pltpu.VMEM((HEAD_A, HEAD_A), F32)], sem=("parallel", "arbitrary"),
        args=(q, k, v, gates, tm_all, s_all, do), comm=comm)


def _onorm_fwd(o, gate, g, name):
    t = o.shape[0]

    def body(o_ref, gate_ref, g_ref, y_ref):
        ov, gv = o_ref[...], gate_ref[...]
        r = lax.rsqrt(jnp.mean(ov * ov, axis=-1, keepdims=True) + RMS_EPS)
        y_ref[...] = (ov * r * g_ref[...] * gv * _sigmoid(gv)).astype(BF16)

    blk = pl.BlockSpec((t, HEAD_A), lambda j: (0, j))
    return pl.pallas_call(
        body, name=name, grid=(N_HEADS_A,), in_specs=[blk, blk, pl.BlockSpec((1, HEAD_A), lambda j: (0, 0))],
        out_specs=blk, out_shape=jax.ShapeDtypeStruct((t, WIDTH_A), BF16),
        compiler_params=_params(("parallel",)))(o, gate, g)


def _onorm_bwd(o, gate, g, dy, name):
    t = o.shape[0]

    def body(o_ref, gate_ref, g_ref, dy_ref, do_ref, dgate_ref, dg_ref):
        @pl.when(pl.program_id(0) == 0)
        def _():
            dg_ref[...] = jnp.zeros_like(dg_ref)

        ov, gv, dyv = o_ref[...], gate_ref[...], dy_ref[...].astype(F32)
        r = lax.rsqrt(jnp.mean(ov * ov, axis=-1, keepdims=True) + RMS_EPS)
        oh = ov * r
        sg, dsg = _silu_and_grad(gv)
        dgate_ref[...] = (dyv * oh * g_ref[...] * dsg).astype(BF16)
        dn = dyv * sg
        dg_ref[...] += jnp.sum(dn * oh, axis=0, keepdims=True)
        dng = dn * g_ref[...]
        do_ref[...] = r * (dng - oh * jnp.mean(dng * oh, axis=-1, keepdims=True))

    blk = pl.BlockSpec((t, HEAD_A), lambda j: (0, j))
    vec = pl.BlockSpec((1, HEAD_A), lambda j: (0, 0))
    return pl.pallas_call(
        body, name=name, grid=(N_HEADS_A,), in_specs=[blk, blk, vec, blk], out_specs=[blk, blk, vec],
        out_shape=[jax.ShapeDtypeStruct((t, WIDTH_A), F32), jax.ShapeDtypeStruct((t, WIDTH_A), BF16),
                   jax.ShapeDtypeStruct((1, HEAD_A), F32)],
        compiler_params=_params(("arbitrary",)))(o, gate, g, dy)


def _cmul(ar, ai, br, bi):
    return ar * br - ai * bi, ar * bi + ai * br


def _scan_tables(ar, ai, reverse):
    p1 = (ar, ai)
    p2 = _cmul(*p1, *p1)
    p4 = _cmul(*p2, *p2)
    p8 = _cmul(*p4, *p4)
    p3 = _cmul(*p2, *p1)
    p5 = _cmul(*p4, *p1)
    p6 = _cmul(*p4, *p2)
    p7 = _cmul(*p4, *p3)
    pows = [p1, p2, p3, p4, p5, p6, p7, p8]
    rows = lax.broadcasted_iota(jnp.int32, (8, ar.shape[1]), 0)
    tr = jnp.zeros((8, ar.shape[1]), F32)
    ti = jnp.zeros((8, ar.shape[1]), F32)
    for r in range(8):
        pw = pows[7 - r] if reverse else pows[r]
        tr = jnp.where(rows == r, pw[0], tr)
        ti = jnp.where(rows == r, pw[1], ti)
    return p1, p2, p4, p8, tr, ti


def _tile_scan(xr, xi, p1, p2, p4, reverse):
    rows = lax.broadcasted_iota(jnp.int32, xr.shape, 0)
    for s, (pr, pi) in ((1, p1), (2, p2), (4, p4)):
        if reverse:
            keep = rows < 8 - s
            sr, si = pltpu.roll(xr, 8 - s, 0), pltpu.roll(xi, 8 - s, 0)
        else:
            keep = rows >= s
            sr, si = pltpu.roll(xr, s, 0), pltpu.roll(xi, s, 0)
        sr, si = jnp.where(keep, sr, 0.0), jnp.where(keep, si, 0.0)
        mr, mi = _cmul(pr, pi, sr, si)
        xr, xi = xr + mr, xi + mi
    return xr, xi


def _s5_scan_fwd(bu, a, name, tb=512, comm=None):
    t = bu.shape[0]
    cb = SCAN_CB
    nt = t // tb

    def body(b_ref, a_ref, x_ref, carry):
        @pl.when(pl.program_id(1) == 0)
        def _():
            carry[...] = jnp.zeros_like(carry)

        ar, ai = a_ref[:, 0:cb], a_ref[:, cb:2 * cb]
        p1, p2, p4, p8, tr, ti = _scan_tables(ar, ai, False)

        def step(j, c):
            cr, ci = c
            i = pl.multiple_of(j * 8, 8)
            xr, xi = _tile_scan(b_ref[pl.ds(i, 8), 0:cb], b_ref[pl.ds(i, 8), cb:2 * cb], p1, p2, p4, False)
            mr, mi = _cmul(tr, ti, cr, ci)
            xr, xi = xr + mr, xi + mi
            x_ref[pl.ds(i, 8), 0:cb] = xr
            x_ref[pl.ds(i, 8), cb:2 * cb] = xi
            return xr[7:8, :], xi[7:8, :]

        cr, ci = lax.fori_loop(0, tb // 8, step, (carry[0:1, :], carry[1:2, :]), unroll=2)
        carry[0:1, :] = cr
        carry[1:2, :] = ci

    blk = pl.BlockSpec((tb, 2 * cb), lambda j, i: (i, j))
    return _call(
        body, name=name, grid=(SSM_CH // cb, nt),
        in_specs=[blk, pl.BlockSpec((1, 2 * cb), lambda j, i: (0, j))], out_specs=blk,
        out_shape=jax.ShapeDtypeStruct((t, 2 * SSM_CH), F32), scratch_shapes=[pltpu.VMEM((8, cb), F32)],
        sem=("parallel", "arbitrary"), args=(bu, a), comm=comm)


def _s5_scan_bwd(dx, x, a, name, tb=512, comm=None):
    t = dx.shape[0]
    cb = SCAN_CB
    nt = t // tb
    nj = tb // 8

    def body(d_ref, x_ref, xp_ref, a_ref, l_ref, da_ref, carry, acc):
        tblk = pl.program_id(1)

        @pl.when(tblk == 0)
        def _():
            carry[...] = jnp.zeros_like(carry)
            acc[...] = jnp.zeros_like(acc)

        ar, ai = a_ref[:, 0:cb], a_ref[:, cb:2 * cb]
        p1, p2, p4, p8, tr, ti = _scan_tables(ar, -ai, True)
        rows = lax.broadcasted_iota(jnp.int32, (8, cb), 0)

        def step(jj, c):
            cr, ci, sr_acc, si_acc = c
            j = nj - 1 - jj
            i = pl.multiple_of(j * 8, 8)
            lr, li = _tile_scan(d_ref[pl.ds(i, 8), 0:cb], d_ref[pl.ds(i, 8), cb:2 * cb], p1, p2, p4, True)
            mr, mi = _cmul(tr, ti, cr, ci)
            lr, li = lr + mr, li + mi
            l_ref[pl.ds(i, 8), 0:cb] = lr
            l_ref[pl.ds(i, 8), cb:2 * cb] = li
            ip = pl.multiple_of(jnp.maximum(j - 1, 0) * 8, 8)
            prev_r = jnp.where(j > 0, x_ref[pl.ds(ip, 8), 0:cb], xp_ref[:, 0:cb])
            prev_i = jnp.where(j > 0, x_ref[pl.ds(ip, 8), cb:2 * cb], xp_ref[:, cb:2 * cb])
            edge = jnp.where(jnp.logical_and(j == 0, tblk == nt - 1), 0.0, 1.0)
            xs_r = jnp.where(rows == 0, pltpu.roll(prev_r, 1, 0) * edge, pltpu.roll(x_ref[pl.ds(i, 8), 0:cb], 1, 0))
            xs_i = jnp.where(rows == 0, pltpu.roll(prev_i, 1, 0) * edge, pltpu.roll(x_ref[pl.ds(i, 8), cb:2 * cb], 1, 0))
            sr_acc = sr_acc + lr * xs_r + li * xs_i
            si_acc = si_acc + li * xs_r - lr * xs_i
            return lr[0:1, :], li[0:1, :], sr_acc, si_acc

        cr, ci, sr_acc, si_acc = lax.fori_loop(
            0, nj, step, (carry[0:1, :], carry[1:2, :], acc[:, 0:cb], acc[:, cb:2 * cb]))
        carry[0:1, :] = cr
        carry[1:2, :] = ci
        acc[:, 0:cb] = sr_acc
        acc[:, cb:2 * cb] = si_acc

        @pl.when(tblk == nt - 1)
        def _():
            da_ref[...] = jnp.sum(acc[...], axis=0, keepdims=True)

    blk = pl.BlockSpec((tb, 2 * cb), lambda j, i: (nt - 1 - i, j))
    prev = pl.BlockSpec((8, 2 * cb), lambda j, i: (jnp.maximum((nt - 1 - i) * (tb // 8) - 1, 0), j))
    vec = pl.BlockSpec((1, 2 * cb), lambda j, i: (0, j))
    return _call(
        body, name=name, grid=(SSM_CH // cb, nt), in_specs=[blk, blk, prev, vec], out_specs=[blk, vec],
        out_shape=[jax.ShapeDtypeStruct((t, 2 * SSM_CH), F32), jax.ShapeDtypeStruct((1, 2 * SSM_CH), F32)],
        scratch_shapes=[pltpu.VMEM((8, cb), F32), pltpu.VMEM((8, 2 * cb), F32)],
        sem=("parallel", "arbitrary"), args=(dx, x, x, a), comm=comm)


def _glu_fwd(yc, u, dvec, wg, bg, name, tr=256):
    t = yc.shape[0]

    def body(yc_ref, u_ref, d_ref, w_ref, b_ref, yl_ref, yb_ref):
        yl = yc_ref[...] + d_ref[...] * u_ref[...]
        yl_ref[...] = yl
        yg, _ = _gelu_and_grad(yl)
        z = jnp.dot(yg.astype(BF16), w_ref[...], preferred_element_type=F32) + b_ref[...]
        yb_ref[...] = (yg * _sigmoid(z)).astype(BF16)

    blk = pl.BlockSpec((tr, SSM_WIDTH), lambda i: (i, 0))
    vec = pl.BlockSpec((1, SSM_WIDTH), lambda i: (0, 0))
    return pl.pallas_call(
        body, name=name, grid=(t // tr,),
        in_specs=[blk, blk, vec, pl.BlockSpec((SSM_WIDTH, SSM_WIDTH), lambda i: (0, 0)), vec],
        out_specs=[blk, blk],
        out_shape=[jax.ShapeDtypeStruct((t, SSM_WIDTH), F32), jax.ShapeDtypeStruct((t, SSM_WIDTH), BF16)],
        compiler_params=_params(("parallel",)))(yc, u, dvec, wg, bg)


def _glu_bwd(yl, u, dvec, wg, bg, dyb, name, tr=256):
    t = yl.shape[0]

    def body(yl_ref, u_ref, d_ref, w_ref, b_ref, dy_ref, dyl_ref, du_ref, dw_ref, db_ref, dd_ref):
        @pl.when(pl.program_id(0) == 0)
        def _():
            dw_ref[...] = jnp.zeros_like(dw_ref)
            db_ref[...] = jnp.zeros_like(db_ref)
            dd_ref[...] = jnp.zeros_like(dd_ref)

        ylv, dyv, wv = yl_ref[...], dy_ref[...].astype(F32), w_ref[...]
        yg, dgelu = _gelu_and_grad(ylv)
        ygb = yg.astype(BF16)
        z = jnp.dot(ygb, wv, preferred_element_type=F32) + b_ref[...]
        sg = _sigmoid(z)
        dz = dyv * yg * sg * (1.0 - sg)
        dzb = dz.astype(BF16)
        dyg = dyv * sg + lax.dot_general(dzb, wv, (((1,), (1,)), ((), ())), preferred_element_type=F32)
        dyl = dyg * dgelu
        dyl_ref[...] = dyl.astype(BF16)
        du_ref[...] = dyl * d_ref[...]
        dw_ref[...] += lax.dot_general(ygb, dzb, (((0,), (0,)), ((), ())), preferred_element_type=F32)
        db_ref[...] += jnp.sum(dz, axis=0, keepdims=True)
        dd_ref[...] += jnp.sum(dyl * u_ref[...], axis=0, keepdims=True)

    blk = pl.BlockSpec((tr, SSM_WIDTH), lambda i: (i, 0))
    vec = pl.BlockSpec((1, SSM_WIDTH), lambda i: (0, 0))
    wsp = pl.BlockSpec((SSM_WIDTH, SSM_WIDTH), lambda i: (0, 0))
    return pl.pallas_call(
        body, name=name, grid=(t // tr,), in_specs=[blk, blk, vec, wsp, vec, blk],
        out_specs=[blk, blk, wsp, vec, vec],
        out_shape=[jax.ShapeDtypeStruct((t, SSM_WIDTH), BF16), jax.ShapeDtypeStruct((t, SSM_WIDTH), F32),
                   jax.ShapeDtypeStruct((SSM_WIDTH, SSM_WIDTH), F32), jax.ShapeDtypeStruct((1, SSM_WIDTH), F32),
                   jax.ShapeDtypeStruct((1, SSM_WIDTH), F32)],
        compiler_params=_params(("arbitrary",)))(yl, u, dvec, wg, bg, dyb)


def _mesh_pos():
    return lax.axis_index("x"), lax.axis_index("y"), lax.axis_index("c")


def _device_index():
    x, y, c = _mesh_pos()
    return 4 * x + 2 * y + c


def _gather_comm(arrays):
    na = len(arrays)

    def own_copy(ins, outs, sems, ai):
        return pltpu.make_async_copy(ins[ai], outs[ai].at[_device_index()], sems[2].at[ai])

    def ctx(ins, outs, sems):
        send_sems, recv_sems = sems[:2]
        x, y, c = _mesh_pos()
        chips = [(1 - x, y), (x, 1 - y), (1 - x, 1 - y)]

        def copy(ai, kk, block, to, own=False):
            slot = outs[ai].at[4 * block[0] + 2 * block[1] + block[2]]
            return pltpu.make_async_remote_copy(
                src_ref=ins[ai] if own else slot, dst_ref=slot, send_sem=send_sems.at[ai, kk],
                recv_sem=recv_sems.at[ai, kk], device_id=to, device_id_type=MESH)

        return (x, y, c), (x, y, 1 - c), chips, c, copy

    def start(ins, outs, sems):
        me, sibling, chips, c, copy = ctx(ins, outs, sems)
        for ai in range(na):
            copy(ai, 0, me, sibling, own=True).start()
            for j, chip in enumerate(chips):
                copy(ai, 1 + j, me, (*chip, c), own=True).start()
        for ai in range(na):
            own_copy(ins, outs, sems, ai).start()

    def mid(ins, outs, sems):
        me, sibling, chips, c, copy = ctx(ins, outs, sems)
        for ai in range(na):
            for j, chip in enumerate(chips):
                copy(ai, 1 + j, (*chip, c), me).wait_recv()
                copy(ai, 4 + j, (*chip, c), sibling).start()

    def end(ins, outs, sems):
        me, sibling, chips, c, copy = ctx(ins, outs, sems)
        for ai in range(na):
            copy(ai, 0, sibling, me).wait_recv()
            copy(ai, 0, me, sibling, own=True).wait_send()
            for j, chip in enumerate(chips):
                copy(ai, 4 + j, (*chip, 1 - c), me).wait_recv()
                copy(ai, 1 + j, me, (*chip, c), own=True).wait_send()
                copy(ai, 4 + j, (*chip, c), sibling).wait_send()
            own_copy(ins, outs, sems, ai).wait()

    return Comm(arrays, [jax.ShapeDtypeStruct((N_DEV,) + a.shape, a.dtype) for a in arrays],
                [pltpu.SemaphoreType.DMA((na, 7)), pltpu.SemaphoreType.DMA((na, 7)), pltpu.SemaphoreType.DMA((na,))],
                start, end, mid)


def _swap_comm(arrays):
    na = len(arrays)
    offs = np.concatenate([[0], np.cumsum([a.shape[1] for a in arrays])]).astype(int)

    def copies(ins, outs, sems):
        x, y, c = _mesh_pos()
        return [pltpu.make_async_remote_copy(
            src_ref=ins[ai].at[2 * k + 1 - c], dst_ref=outs[0].at[k, pl.ds(int(offs[ai]), arrays[ai].shape[1])],
            send_sem=sems[0].at[ai, k], recv_sem=sems[1].at[ai, k], device_id=(x, y, 1 - c), device_id_type=MESH)
            for ai in range(na) for k in range(4)]

    def start(ins, outs, sems):
        for cp in copies(ins, outs, sems):
            cp.start()

    def end(ins, outs, sems):
        for cp in copies(ins, outs, sems):
            cp.wait()

    return Comm(arrays, [jax.ShapeDtypeStruct((4, int(offs[-1]), PACK_COLS), arrays[0].dtype)],
                [pltpu.SemaphoreType.DMA((na, 4)), pltpu.SemaphoreType.DMA((na, 4))], start, end)


def _chips_comm(send):
    def copies(ins, outs, sems):
        x, y, c = _mesh_pos()
        chips = [(1 - x, y), (x, 1 - y), (1 - x, 1 - y)]
        return [pltpu.make_async_remote_copy(
            src_ref=ins[0].at[2 * cx + cy], dst_ref=outs[0].at[j], send_sem=sems[0].at[j], recv_sem=sems[1].at[j],
            device_id=(cx, cy, c), device_id_type=MESH) for j, (cx, cy) in enumerate(chips)]

    def start(ins, outs, sems):
        for cp in copies(ins, outs, sems):
            cp.start()

    def end(ins, outs, sems):
        for cp in copies(ins, outs, sems):
            cp.wait()

    return Comm([send], [jax.ShapeDtypeStruct((3,) + send.shape[1:], send.dtype)],
                [pltpu.SemaphoreType.DMA((3,)), pltpu.SemaphoreType.DMA((3,))], start, end)


def _all_gather(arrays, name):
    na = len(arrays)

    def body(*refs):
        ins, outs = refs[:na], refs[na:2 * na]
        send_sems, recv_sems, local_sems = refs[2 * na:]
        x, y, c = _mesh_pos()
        me, sibling = (x, y, c), (x, y, 1 - c)
        chips = [(1 - x, y), (x, 1 - y), (1 - x, 1 - y)]
        waits = []
        for ai in range(na):
            in_ref, out_ref = ins[ai], outs[ai]

            def slot(px, py, pc, out_ref=out_ref):
                return out_ref.at[4 * px + 2 * py + pc]

            def copy(kk, block, to, src=None, ai=ai, slot=slot):
                return pltpu.make_async_remote_copy(
                    src_ref=slot(*block) if src is None else src, dst_ref=slot(*block),
                    send_sem=send_sems.at[ai, kk], recv_sem=recv_sems.at[ai, kk], device_id=to, device_id_type=MESH)

            mine = pltpu.make_async_copy(in_ref, slot(*me), local_sems.at[ai])
            mine.start()
            first = [copy(0, me, sibling, src=in_ref)]
            first += [copy(1 + j, me, (*chip, c), src=in_ref) for j, chip in enumerate(chips)]
            for cp in first:
                cp.start()
            waits.append((copy, mine, first))
        sends = []
        for ai in range(na):
            copy, mine, first = waits[ai]
            passed = [copy(4 + j, (*chip, c), sibling) for j, chip in enumerate(chips)]
            for j, chip in enumerate(chips):
                copy(1 + j, (*chip, c), me).wait_recv()
                passed[j].start()
            sends.append(passed)
        for ai in range(na):
            copy, mine, first = waits[ai]
            copy(0, sibling, me).wait_recv()
            for j, chip in enumerate(chips):
                copy(4 + j, (*chip, 1 - c), me).wait_recv()
            for cp in first + sends[ai]:
                cp.wait_send()
            mine.wait()

    any_spec = pl.BlockSpec(memory_space=pl.ANY)
    return pl.pallas_call(
        body, name=name, in_specs=[any_spec] * na, out_specs=[any_spec] * na,
        out_shape=[jax.ShapeDtypeStruct((N_DEV,) + a.shape, a.dtype) for a in arrays],
        scratch_shapes=[pltpu.SemaphoreType.DMA((na, 7)), pltpu.SemaphoreType.DMA((na, 7)),
                        pltpu.SemaphoreType.DMA((na,))],
        compiler_params=pltpu.CompilerParams(has_side_effects=True))(*arrays)


def _swap_sibling(arrays, name):
    na = len(arrays)
    offs = np.concatenate([[0], np.cumsum([a.shape[1] for a in arrays])]).astype(int)
    rows = int(offs[-1])

    def body(*refs):
        ins, recv_ref = refs[:na], refs[na]
        send_sems, recv_sems = refs[na + 1:]
        x, y, c = _mesh_pos()
        started = []
        for ai in range(na):
            span = pl.ds(int(offs[ai]), arrays[ai].shape[1])
            for k in range(4):
                remote = pltpu.make_async_remote_copy(
                    src_ref=ins[ai].at[2 * k + 1 - c], dst_ref=recv_ref.at[k, span], send_sem=send_sems.at[ai, k],
                    recv_sem=recv_sems.at[ai, k], device_id=(x, y, 1 - c), device_id_type=MESH)
                remote.start()
                started.append(remote)
        for remote in started:
            remote.wait()

    any_spec = pl.BlockSpec(memory_space=pl.ANY)
    return pl.pallas_call(
        body, name=name, in_specs=[any_spec] * na, out_specs=any_spec,
        out_shape=jax.ShapeDtypeStruct((4, rows, PACK_COLS), arrays[0].dtype),
        scratch_shapes=[pltpu.SemaphoreType.DMA((na, 4)), pltpu.SemaphoreType.DMA((na, 4))])(*arrays)


def _exchange_chips(send, name):
    def body(s_ref, o_ref, send_sems, recv_sems):
        x, y, c = _mesh_pos()
        chips = [(1 - x, y), (x, 1 - y), (1 - x, 1 - y)]
        cps = [pltpu.make_async_remote_copy(
            src_ref=s_ref.at[2 * cx + cy], dst_ref=o_ref.at[j], send_sem=send_sems.at[j], recv_sem=recv_sems.at[j],
            device_id=(cx, cy, c), device_id_type=MESH) for j, (cx, cy) in enumerate(chips)]
        for cp in cps:
            cp.start()
        for cp in cps:
            cp.wait()

    any_spec = pl.BlockSpec(memory_space=pl.ANY)
    return pl.pallas_call(
        body, name=name, in_specs=[any_spec], out_specs=any_spec,
        out_shape=jax.ShapeDtypeStruct((3,) + send.shape[1:], send.dtype),
        scratch_shapes=[pltpu.SemaphoreType.DMA((3,)), pltpu.SemaphoreType.DMA((3,))])(send)


def _pair_sum(keep, recv, name, tr=464):
    nchip, rows, cols = keep.shape

    def body(g_ref, r_ref, o_ref):
        o_ref[...] = (g_ref[...].astype(F32) + r_ref[...].astype(F32)).astype(BF16)

    blk = pl.BlockSpec((1, tr, cols), lambda k, i: (k, i, 0))
    return pl.pallas_call(
        body, name=name, grid=(nchip, rows // tr), in_specs=[blk, blk], out_specs=blk,
        out_shape=jax.ShapeDtypeStruct((nchip, rows, cols), BF16),
        compiler_params=_params(("parallel", "parallel")))(keep, recv)


def _chip_sum(own, others, name, tr=464):
    _, rows, cols = own.shape
    chip = (2 * lax.axis_index("x") + lax.axis_index("y")).astype(jnp.int32).reshape(1)

    def body(chip_ref, own_ref, oth_ref, o_ref):
        del chip_ref
        acc = own_ref[0].astype(F32)
        for j in range(3):
            acc = acc + oth_ref[j].astype(F32)
        o_ref[...] = acc

    grid_spec = pltpu.PrefetchScalarGridSpec(
        num_scalar_prefetch=1, grid=(rows // tr,),
        in_specs=[pl.BlockSpec((1, tr, cols), lambda i, chip_ref: (chip_ref[0], i, 0)),
                  pl.BlockSpec((3, tr, cols), lambda i, chip_ref: (0, i, 0))],
        out_specs=pl.BlockSpec((tr, cols), lambda i, chip_ref: (i, 0)))
    return pl.pallas_call(
        body, name=name, grid_spec=grid_spec, out_shape=jax.ShapeDtypeStruct((rows, cols), F32),
        compiler_params=_params(("parallel",)))(chip, own, others)


def _sum_leading(parts, name, tr=464):
    nparts, rows, cols = parts.shape
    tr = tr if rows % tr == 0 else rows

    def body(p_ref, o_ref):
        acc = p_ref[0].astype(F32)
        for i in range(1, nparts):
            acc = acc + p_ref[i].astype(F32)
        o_ref[...] = acc

    return pl.pallas_call(
        body, name=name, grid=(rows // tr,),
        in_specs=[pl.BlockSpec((nparts, tr, cols), lambda i: (0, i, 0))],
        out_specs=pl.BlockSpec((tr, cols), lambda i: (i, 0)), out_shape=jax.ShapeDtypeStruct((rows, cols), F32),
        compiler_params=_params(("parallel",)))(parts)


def _adamw(w, g, m, v, name):
    shape = w.shape
    cols = shape[-1]
    rows = int(np.prod(shape[:-1])) if len(shape) > 1 else 1
    w2, g2, m2, v2 = (a.reshape(rows, cols) for a in (w, g, m, v))
    tr = rows
    for cand in (512, 256, 128, 64, 32, 16, 8):
        if rows % cand == 0 and rows > cand:
            tr = cand
            break
    bc1, bc2 = 1.0 - ADAM_B1 ** ADAM_STEP, 1.0 - ADAM_B2 ** ADAM_STEP

    def body(w_ref, g_ref, m_ref, v_ref, d_ref, nm_ref, nv_ref):
        gv = g_ref[...]
        nm = ADAM_B1 * m_ref[...] + (1.0 - ADAM_B1) * gv
        nv = ADAM_B2 * v_ref[...] + (1.0 - ADAM_B2) * (gv * gv)
        nm_ref[...] = nm
        nv_ref[...] = nv
        d_ref[...] = -ADAM_LR * ((nm / bc1) / (jnp.sqrt(nv / bc2) + ADAM_EPS) + ADAM_WD * w_ref[...])

    blk = pl.BlockSpec((tr, cols), lambda i: (i, 0))
    outs = pl.pallas_call(
        body, name=name, grid=(rows // tr,), in_specs=[blk] * 4, out_specs=[blk] * 3,
        out_shape=[jax.ShapeDtypeStruct((rows, cols), F32)] * 3, compiler_params=_params(("parallel",)))(w2, g2, m2, v2)
    return tuple(o.reshape(shape) for o in outs)


WEIGHT_NAMES = ['norm_mix_g', 'norm_xa_g', 'norm_ffn_g', 'norm_mem_g', 'norm_final_g', 'w_in_ab', 'conv_qkv_a',
                'a_log_a', 'dt_bias_a', 'onorm_g_a', 'ssm_lambda_re', 'ssm_lambda_im', 'ssm_b_re', 'ssm_b_im',
                'ssm_c_re', 'ssm_c_im', 'ssm_d', 'ssm_log_dt', 'w_glu_b', 'b_glu_b', 'w_out_ab', 'pool_w',
                'pool_scale', 'xa_wq', 'xa_wkv', 'xa_wo', 'ffn_w_up', 'ffn_conv', 'ffn_w_down']
BIG_SHARDED = {'w_in_ab': ((1, 1024, 2568), 2), 'w_glu_b': ((1, 512, 512), 1), 'w_out_ab': ((1, 1024, 1024), 1),
               'pool_w': ((1, 4, 256, 256), 2), 'xa_wq': ((2, 1024, 1024), 1), 'xa_wkv': ((2, 1024, 2048), 2),
               'xa_wo': ((2, 1024, 1024), 1), 'ffn_w_up': ((2, 1024, 5632), 2), 'ffn_w_down': ((2, 2816, 1024), 1)}
SMALL_SHARDED = {'conv_qkv_a': ((1, 4, 1536), 2), 'pool_scale': ((1, 1024), 1), 'ffn_conv': ((2, 3, 5632), 2)}
REPLICATED = {'norm_mix_g': (2, 1024), 'norm_xa_g': (2, 1024), 'norm_ffn_g': (2, 1024), 'norm_mem_g': (1024,),
              'norm_final_g': (1024,), 'a_log_a': (1, 4), 'dt_bias_a': (1, 4), 'onorm_g_a': (1, 128),
              'ssm_lambda_re': (1, 32, 64), 'ssm_lambda_im': (1, 32, 64), 'ssm_b_re': (1, 32, 64, 16),
              'ssm_b_im': (1, 32, 64, 16), 'ssm_c_re': (1, 32, 16, 64), 'ssm_c_im': (1, 32, 16, 64),
              'ssm_d': (1, 32, 16), 'ssm_log_dt': (1, 32), 'b_glu_b': (1, 512)}
PACK_ROW_ALIGN = 8


def _shard_shape(shape, axis):
    return tuple(s // N_DEV if i == axis else s for i, s in enumerate(shape))


def _round_up(n, m):
    return (n + m - 1) // m * m


def _pack(arrays):
    total = sum(int(np.prod(a.shape)) for a in arrays)
    padded = _round_up(total, PACK_COLS * PACK_ROW_ALIGN)
    parts = [a.astype(F32).reshape(-1) for a in arrays]
    if padded != total:
        parts.append(jnp.zeros((padded - total,), F32))
    return jnp.concatenate(parts).reshape(padded // PACK_COLS, PACK_COLS)


def _unpack(packed, shapes):
    flat, out, off = packed.reshape(-1), [], 0
    for shape in shapes:
        size = int(np.prod(shape))
        out.append(flat[off:off + size].reshape(shape))
        off += size
    return out


def _split_shards(full, axis):
    shape = full.shape
    s = shape[axis] // N_DEV
    a = full.reshape(shape[:axis] + (N_DEV, s) + shape[axis + 1:])
    return jnp.moveaxis(a, axis, 0).reshape(N_DEV, -1)


def _merge_shards(pieces, shape, axis):
    sh = _shard_shape(shape, axis)
    a = pieces.reshape((N_DEV,) + sh)
    a = jnp.moveaxis(a, 0, axis)
    return a.reshape(shape)


_SCAN_NB = SSM_CH // SCAN_CB


def _to_scan_layout(m, axis):
    shape = m.shape
    m = m.reshape(shape[:axis] + (2, _SCAN_NB, SCAN_CB) + shape[axis + 1:])
    return jnp.swapaxes(m, axis, axis + 1).reshape(shape)


def _from_scan_layout(m, axis):
    shape = m.shape
    m = m.reshape(shape[:axis] + (_SCAN_NB, 2, SCAN_CB) + shape[axis + 1:])
    return jnp.swapaxes(m, axis, axis + 1).reshape(shape)


def _s5_discretise(lam_re, lam_im, b_re, b_im, log_dt):
    dt = jnp.exp(log_dt)[:, None]
    mag = jnp.exp(lam_re * dt)
    ang = lam_im * dt
    lb_re, lb_im = mag * jnp.cos(ang), mag * jnp.sin(ang)
    den = lam_re * lam_re + lam_im * lam_im
    nr, ni = lb_re - 1.0, lb_im
    coef_re = (nr * lam_re + ni * lam_im) / den
    coef_im = (ni * lam_re - nr * lam_im) / den
    bb_re = coef_re[..., None] * b_re - coef_im[..., None] * b_im
    bb_im = coef_re[..., None] * b_im + coef_im[..., None] * b_re
    return lb_re, lb_im, bb_re, bb_im


_GROUPS_PER_BLOCK = N_GROUPS // _SCAN_NB
_U_BLOCK = _GROUPS_PER_BLOCK * SSM_GROUP


def _s5_matrices(lb_re, lb_im, bb_re, bb_im, c_re, c_im):
    eye = jnp.eye(_GROUPS_PER_BLOCK, dtype=F32)
    blocked = lambda m: m.reshape((_SCAN_NB, _GROUPS_PER_BLOCK) + m.shape[1:])
    bmat = lambda bb: jnp.einsum('jgph,gk->jghkp', blocked(bb), eye).reshape(_SCAN_NB, _U_BLOCK, SCAN_CB)
    cmat = lambda cc: jnp.einsum('jghp,gk->jkpgh', blocked(cc), eye).reshape(_SCAN_NB, SCAN_CB, _U_BLOCK)
    b_in = jnp.concatenate([bmat(bb_re), bmat(bb_im)], axis=2)
    c_out = jnp.concatenate([cmat(c_re), -cmat(c_im)], axis=1)
    a_row = _to_scan_layout(jnp.concatenate([lb_re.reshape(1, SSM_CH), lb_im.reshape(1, SSM_CH)], axis=1), 1)
    return b_in, c_out, a_row


def _s5_matrix_grads(db_in, dc_out, da_row):
    da_nat = _from_scan_layout(da_row, 1)
    eye = jnp.eye(_GROUPS_PER_BLOCK, dtype=F32)
    nb, gb = _SCAN_NB, _GROUPS_PER_BLOCK
    bgrad = lambda m: jnp.einsum('jghkp,gk->jgph', m.reshape(nb, gb, SSM_GROUP, gb, SSM_STATE), eye
                                 ).reshape(N_GROUPS, SSM_STATE, SSM_GROUP)
    cgrad = lambda m: jnp.einsum('jkpgh,gk->jghp', m.reshape(nb, gb, SSM_STATE, gb, SSM_GROUP), eye
                                 ).reshape(N_GROUPS, SSM_GROUP, SSM_STATE)
    dbb_re, dbb_im = bgrad(db_in[:, :, :SCAN_CB]), bgrad(db_in[:, :, SCAN_CB:])
    dc_re, dc_im = cgrad(dc_out[:, :SCAN_CB]), -cgrad(dc_out[:, SCAN_CB:])
    dlb_re = da_nat[0, :SSM_CH].reshape(N_GROUPS, SSM_STATE)
    dlb_im = da_nat[0, SSM_CH:].reshape(N_GROUPS, SSM_STATE)
    return dlb_re, dlb_im, dbb_re, dbb_im, dc_re, dc_im


def _as_pieces(a):
    return a.reshape(N_DEV, a.shape[0] // N_DEV, a.shape[1])


def _hybrid_fwd(xn, x, wts, p, weights, riders):
    sv = {}
    hq = _mm(xn, wts['w_qkv_t'], "nt", "l0_in_qkv")
    gate = _mm(xn, wts['w_gate_t'], "nt", "l0_in_gate")
    ba = _mm(xn, wts['w_ba_t'], "nt", "l0_in_ba")
    u = _mm(xn, wts['w_u_t'], "nt", "l0_in_u")
    conv = p['conv_qkv']
    q = _qkv_pre_fwd(hq, conv, 0, 4, True, HEAD_A ** -0.5, "l0_q_pre")
    k = _qkv_pre_fwd(hq, conv, 4, 4, True, 1.0, "l0_k_pre")
    v = _qkv_pre_fwd(hq, conv, 8, 4, False, 1.0, "l0_v_pre")
    gates = _gates_fwd(ba, p['arow'], p['brow'], "l0_gates")
    o, tm_all, s_all = riders.run("l0_gdr_fwd", _gdr_fwd, q, k, v, gates)
    wts['w_glu'], wts['w_out'] = weights.full['w_glu'], weights.full['w_out']
    y_a = _onorm_fwd(o, gate, p['onorm_g'], "l0_onorm")
    bu = riders.run("l0_s5_bu", _mm_bd, u, p['b_in'], "nn")
    xs = riders.run("l0_s5_scan", _s5_scan_fwd, bu, p['a_row'])
    yc = riders.run("l0_s5_cx", _mm_bd, xs, p['c_out'], "nn")
    yl, y_b = _glu_fwd(yc, u, p['d_row'], wts['w_glu'], p['b_glu'], "l0_glu")
    mixed = jnp.concatenate([y_a, y_b], axis=1)
    x1 = _mm(mixed, wts['w_out'], "nn", "l0_out", res=x)
    sv.update(hq=hq, gate=gate, ba=ba, u=u, q=q, k=k, v=v, gb=gates, o=o, tm=tm_all, s=s_all, xs=xs, yl=yl, mixed=mixed)
    return x1, sv


def _hybrid_bwd(dx1, xn, wts, p, sv, riders):
    gr = {}
    dmixed = _mm(dx1, wts['w_out'], "nt", "l0_out_dx", out_dtype=BF16)
    riders.grad('w_out', _as_pieces(_mm(sv['mixed'], dx1, "tn", "l0_out_dw", out_dtype=BF16)))
    dya, dyb = dmixed[:, :WIDTH_A], dmixed[:, WIDTH_A:]
    dyl, du_direct, dw_glu, gr['b_glu_b'], dd = _glu_bwd(
        sv['yl'], sv['u'], p['d_row'], wts['w_glu'], p['b_glu'], dyb, "l0_glu_bwd")
    riders.grad('w_glu', dw_glu.astype(BF16).reshape(N_DEV, -1, PACK_COLS))
    dxs = riders.run("l0_s5_cx_dx", _mm_bd, dyl, p['c_out'], "nt")
    dc_out = _mm_bd(sv['xs'], dyl, "tn", "l0_s5_cx_dw")
    lam, da_row = riders.run("l0_s5_scan_bwd", _s5_scan_bwd, dxs, sv['xs'], p['a_row'])
    du = _mm_bd(lam, p['b_in'], "nt", "l0_s5_bu_dx", res=du_direct, out_dtype=BF16)
    db_in = _mm_bd(sv['u'], lam, "tn", "l0_s5_bu_dw")
    gr['s5'] = (db_in, dc_out, da_row, dd)
    do, dgate, gr['onorm_g_a'] = _onorm_bwd(sv['o'], sv['gate'], p['onorm_g'], dya, "l0_onorm_bwd")
    dq, dk, dv, dgb = riders.run("l0_gdr_bwd", _gdr_bwd, sv['q'], sv['k'], sv['v'], sv['gb'], sv['tm'], sv['s'], do)
    conv = p['conv_qkv']
    dhq_q, dcw_q = _qkv_pre_bwd(sv['hq'], conv, dq, 0, 4, True, HEAD_A ** -0.5, "l0_q_pre_bwd")
    dhq_k, dcw_k = _qkv_pre_bwd(sv['hq'], conv, dk, 4, 4, True, 1.0, "l0_k_pre_bwd")
    dhq_v, dcw_v = _qkv_pre_bwd(sv['hq'], conv, dv, 8, 4, False, 1.0, "l0_v_pre_bwd")
    gr['conv_qkv_a'] = jnp.concatenate([dcw_q, dcw_k, dcw_v], axis=1)
    dhq = jnp.concatenate([dhq_q, dhq_k, dhq_v], axis=1)
    dba, da_log, ddt_bias = _gates_bwd(sv['ba'], p['arow'], p['brow'], dgb, "l0_gates_bwd")
    gr['a_log_a'], gr['dt_bias_a'] = da_log[:, 4:8], ddt_bias[:, 4:8]
    dxn = _mm(dhq, wts['w_qkv_t'], "nn", "l0_in_qkv_dx")
    dxn = _mm(dgate, wts['w_gate_t'], "nn", "l0_in_gate_dx", res=dxn)
    dxn = _mm(dba, wts['w_ba_t'], "nn", "l0_in_ba_dx", res=dxn)
    dxn = _mm(du, wts['w_u_t'], "nn", "l0_in_u_dx", res=dxn)
    dw_qkv_t = _mm(dhq, xn, "tn", "l0_in_qkv_dw", out_dtype=BF16)
    dw_gate_t = _mm(dgate, xn, "tn", "l0_in_gate_dw", out_dtype=BF16)
    dw_ba_t = _mm(dba, xn, "tn", "l0_in_ba_dw", out_dtype=BF16)
    dw_u_t = _mm(du, xn, "tn", "l0_in_u_dw", out_dtype=BF16)
    dw_in_t = _as_pieces(jnp.concatenate([dw_qkv_t, dw_gate_t, dw_ba_t[:8], dw_u_t], axis=0))
    riders.grad('w_in_t', jnp.concatenate(
        [dw_in_t, jnp.zeros((N_DEV, dict(PIECES)['w_in_t'] - W_IN_PIECE, D_MODEL), BF16)], axis=1))
    return dxn, gr


def _xa_fwd(x1, g, mem_n, wq, wkv_t, wo, tag, riders):
    xq = _rms_fwd(x1, g, BF16, tag + "_norm")
    q = _mm(xq, wq, "nn", tag + "_q", out_dtype=BF16)
    kv = _mm(mem_n, wkv_t, "nt", tag + "_kv", out_dtype=BF16)
    o = riders.run(tag + "_attn", _attn_fwd, q, kv)
    x2 = _mm(o, wo, "nn", tag + "_o", res=x1)
    return x2, dict(xq=xq, q=q, kv=kv, o=o)


def _xa_bwd(dx2, x1, g, mem_n, wq, wkv_t, wo, sv, tag, layer, riders):
    do = _mm(dx2, wo, "nt", tag + "_o_dx", out_dtype=BF16)
    riders.grad('wo%d' % layer, _as_pieces(_mm(sv['o'], dx2, "tn", tag + "_o_dw", out_dtype=BF16)))
    dq, dk, dv = _attn_bwd(sv['q'], sv['kv'], do, tag + "_attn_bwd")
    dkv = jnp.concatenate([dk, dv], axis=1).astype(BF16)
    dxq = _mm(dq, wq, "nt", tag + "_q_dx")
    riders.grad('wq%d' % layer, _as_pieces(_mm(sv['xq'], dq, "tn", tag + "_q_dw", out_dtype=BF16)))
    dmem_n = _mm(dkv, wkv_t, "nn", tag + "_kv_dx")
    riders.grad('wkv_t%d' % layer, _as_pieces(_mm(dkv, mem_n, "tn", tag + "_kv_dw", out_dtype=BF16)))
    dx1, dg = riders.run(tag + "_norm_bwd", _rms_bwd, x1, g, dxq, dx2)
    return dx1, dmem_n, dg


def _ffn_fwd(x2, g, w_up_t, conv, w_down, tag, riders):
    xf = _rms_fwd(x2, g, BF16, tag + "_norm")
    h = riders.run(tag + "_up", _mm, xf, w_up_t, "nt")
    a = riders.run(tag + "_act", _ffn_act_fwd, h, conv)
    x3 = _mm(a, w_down, "nn", tag + "_down", res=x2)
    return x3, dict(xf=xf, h=h, a=a)


def _ffn_bwd(dx3, x2, g, w_up_t, conv, w_down, sv, tag, layer, riders):
    da = _mm(dx3, w_down, "nt", tag + "_down_dx")
    riders.grad('down%d' % layer, _as_pieces(_mm(sv['a'], dx3, "tn", tag + "_down_dw", out_dtype=BF16)))
    dh, dconv = riders.run(tag + "_act_bwd", _ffn_act_bwd, sv['h'], conv, da)
    dxf = riders.run(tag + "_up_dx", _mm, dh, w_up_t, "nn")
    dw_up_t = riders.run(tag + "_up_dw", _mm, dh, sv['xf'], "tn", out_dtype=BF16)
    riders.grad('up_t%d' % layer, _as_pieces(dw_up_t))
    dx2, dg = riders.run(tag + "_norm_bwd", _rms_bwd, x2, g, dxf, dx3)
    return dx2, dconv, dg


BIG_NAMES, SMALL_NAMES, REP_NAMES = list(BIG_SHARDED), list(SMALL_SHARDED), list(REPLICATED)
BIG_SIZES = [int(np.prod(_shard_shape(*BIG_SHARDED[n]))) for n in BIG_NAMES]
SMALL_SIZES = [int(np.prod(_shard_shape(*SMALL_SHARDED[n]))) for n in SMALL_NAMES]


PIECES = [('w_in_t', 384), ('w_glu', 32), ('w_out', 128), ('pool_w', 32), ('wq0', 128), ('wq1', 128),
          ('wkv_t0', 256), ('wkv_t1', 256), ('wo0', 128), ('wo1', 128), ('up_t0', 704), ('up_t1', 704),
          ('down0', 352), ('down1', 352)]
PIECE_OFFS = dict(zip([k for k, _ in PIECES], np.concatenate([[0], np.cumsum([r for _, r in PIECES])[:-1]]).tolist()))
W_IN_ROWS = 4 * WIDTH_A + 2 * N_HEADS_A + SSM_WIDTH
W_IN_PIECE = W_IN_ROWS // N_DEV


def _row_tile(rows):
    return max(t for t in range(16, min(rows, 512) + 1, 16) if rows % t == 0)


class _Riders:
    def __init__(self):
        self.waiting = {}
        self.grads = {}
        self.groups = []
        self.reduced = {}

    def add(self, host, comm, then):
        self.waiting.setdefault(host, []).append((comm, then))

    def run(self, name, fn, *args, **kw):
        riders = self.waiting.pop(name, [])
        if not riders:
            return fn(*args, name=name, **kw)
        out, couts = fn(*args, name=name, comm=[c for c, _ in riders], **kw)
        for (_, then), got in zip(riders, couts):
            then(got)
        return out

    def exchange(self, comm, host, name, then):
        if host is None:
            then(_comm_only(comm, name))
        else:
            self.add(host, comm, then)

    def grad(self, key, pieces):
        self.grads[key] = pieces
        for group in [g for g in self.groups if all(k in self.grads for k in g[1])]:
            self.groups.remove(group)
            self._reduce(*group)

    def _reduce(self, name, keys, swap_host, chips_host):
        arrays = [self.grads[k] for k in keys]
        rows = sum(a.shape[1] for a in arrays)
        tile = _row_tile(rows)

        def after_chips(chip_sums, got):
            total = _chip_sum(chip_sums, got[0], name + "_chip_sum", tr=tile)
            off = 0
            for k, a in zip(keys, arrays):
                self.reduced[k] = total[off:off + a.shape[1]]
                off += a.shape[1]

        def after_swap(got):
            core = lax.axis_index("c")
            keep = jnp.concatenate(
                [lax.dynamic_index_in_dim(a.reshape(4, 2, a.shape[1], PACK_COLS), core, 1, keepdims=False)
                 for a in arrays], axis=1)
            chip_sums = _pair_sum(keep, got[0], name + "_pair_sum", tr=tile)
            self.exchange(_chips_comm(chip_sums), chips_host, name + "_to_chips",
                          functools.partial(after_chips, chip_sums))

        self.exchange(_swap_comm(arrays), swap_host, name + "_to_sibling", after_swap)


class _Weights:
    def __init__(self, inp):
        bf = lambda a: a.astype(BF16)
        local = {'w_in_t': bf(inp['w_in_ab'][0]).T, 'w_glu': bf(inp['w_glu_b'][0]), 'w_out': bf(inp['w_out_ab'][0]),
                 'pool_w': bf(inp['pool_w'][0]),
                 'small': _pack([inp[n] for n in SMALL_NAMES])}
        for l in range(2):
            local['wq%d' % l] = bf(inp['xa_wq'][l])
            local['wkv_t%d' % l] = bf(inp['xa_wkv'][l]).T
            local['wo%d' % l] = bf(inp['xa_wo'][l])
            local['up_t%d' % l] = bf(inp['ffn_w_up'][l]).T
            local['down%d' % l] = bf(inp['ffn_w_down'][l])
        self.local, self.full = local, {}

    def plan(self, keys):
        return _gather_comm([self.local[k] for k in keys])

    def land(self, keys, gathered):
        for k, g in zip(keys, gathered):
            if k == 'small':
                off = 0
                for n, size in zip(SMALL_NAMES, SMALL_SIZES):
                    self.full[n] = _merge_shards(g.reshape(N_DEV, -1)[:, off:off + size], *SMALL_SHARDED[n])
                    off += size
            elif k == 'pool_w':
                self.full[k] = jnp.swapaxes(g, 0, 1).reshape(len(POOL_WINDOWS), POOL_GROUP, POOL_GROUP)
            else:
                self.full[k] = g.reshape(N_DEV * g.shape[1], g.shape[2])


GATHER_FIRST = ['w_in_t', 'small']
GATHER_RIDES = [('l0_gdr_fwd', ['w_glu', 'w_out', 'wq0', 'wkv_t0', 'wo0', 'up_t0']),
                ('l0_s5_bu', ['pool_w', 'wq1']), ('l0_s5_scan', ['down0']), ('l0_s5_cx', ['wo1']),
                ('l0_xa_attn', ['wkv_t1']), ('l0_ffn_up', ['up_t1']), ('l0_ffn_act', ['down1'])]
GRAD_RIDES = [('g_down1', ['down1'], 'l1_ffn_act_bwd', 'l1_ffn_up_dx'),
              ('g_up1', ['up_t1'], 'l1_ffn_norm_bwd', 'l0_ffn_act_bwd'),
              ('g_xa1', ['wq1', 'wkv_t1', 'wo1', 'pool_w'], 'l1_mix_norm_bwd', 'l0_ffn_up_dx'),
              ('g_down0', ['down0'], 'l0_ffn_act_bwd', 'l0_ffn_up_dw'),
              ('g_up0', ['up_t0'], 'l0_ffn_norm_bwd', 'l0_gdr_bwd'),
              ('g_xa0', ['wq0', 'wkv_t0', 'wo0'], 'l0_xa_norm_bwd', 'l0_gdr_bwd'),
              ('g_out', ['w_out', 'w_glu'], 'l0_s5_cx_dx', 'l0_s5_scan_bwd'),
              ('g_in', ['w_in_t'], None, None)]


def _local_step(inp):
    f32_of = lambda n: inp[n].astype(F32)
    weights = _Weights(inp)
    riders = _Riders()
    riders.groups = list(GRAD_RIDES)
    full = weights.full
    weights.land(GATHER_FIRST, _comm_only(weights.plan(GATHER_FIRST), "gather_first"))
    for host, keys in GATHER_RIDES:
        riders.add(host, weights.plan(keys), functools.partial(weights.land, keys))
    w_in_t = full['w_in_t']
    wts0 = dict(w_qkv_t=w_in_t[:3 * WIDTH_A], w_gate_t=w_in_t[3 * WIDTH_A:4 * WIDTH_A],
                w_ba_t=jnp.concatenate([w_in_t[4 * WIDTH_A:4 * WIDTH_A + 8], jnp.zeros((LANE - 8, D_MODEL), BF16)], 0),
                w_u_t=w_in_t[4 * WIDTH_A + 8:])
    lb_disc, disc_vjp = jax.vjp(_s5_discretise, f32_of('ssm_lambda_re')[0], f32_of('ssm_lambda_im')[0],
                                f32_of('ssm_b_re')[0], f32_of('ssm_b_im')[0], f32_of('ssm_log_dt')[0])
    b_in, c_out, a_row = _s5_matrices(*lb_disc, f32_of('ssm_c_re')[0], f32_of('ssm_c_im')[0])
    zeros4 = jnp.zeros((1, 4), F32)
    p0 = dict(conv_qkv=full['conv_qkv_a'][0], onorm_g=f32_of('onorm_g_a'),
              arow=jnp.concatenate([zeros4, f32_of('a_log_a'), jnp.zeros((1, LANE - 8), F32)], 1),
              brow=jnp.concatenate([zeros4, f32_of('dt_bias_a'), jnp.zeros((1, LANE - 8), F32)], 1),
              b_in=b_in.astype(BF16), c_out=c_out.astype(BF16), a_row=a_row,
              d_row=f32_of('ssm_d').reshape(1, SSM_WIDTH), b_glu=f32_of('b_glu_b'))

    x0 = inp['x'][0]
    mem_n = _rms_fwd(inp['mem'][0], inp['norm_mem_g'], BF16, "mem_norm")
    xn0 = _rms_fwd(x0, inp['norm_mix_g'][0], BF16, "l0_mix_norm")
    x1, sv_mix0 = _hybrid_fwd(xn0, x0, wts0, p0, weights, riders)
    x2, sv_xa0 = _xa_fwd(x1, inp['norm_xa_g'][0], mem_n, full['wq0'], full['wkv_t0'], full['wo0'], "l0_xa", riders)
    x3, sv_ffn0 = _ffn_fwd(x2, inp['norm_ffn_g'][0], full['up_t0'], full['ffn_conv'][0], full['down0'], "l0_ffn", riders)
    xn1 = _rms_fwd(x3, inp['norm_mix_g'][1], F32, "l1_mix_norm")
    x4 = _pool_fwd(xn1, full['pool_w'], full['pool_scale'], x3, "l1_pool")
    x5, sv_xa1 = _xa_fwd(x4, inp['norm_xa_g'][1], mem_n, full['wq1'], full['wkv_t1'], full['wo1'], "l1_xa", riders)
    x6, sv_ffn1 = _ffn_fwd(x5, inp['norm_ffn_g'][1], full['up_t1'], full['ffn_conv'][1], full['down1'], "l1_ffn", riders)
    loss_part, dx6, dg_final = _loss_head(x6, inp['norm_final_g'], inp['loss_target'][0], "loss_head")

    dx5, dconv1, dg_ffn1 = _ffn_bwd(dx6, x5, inp['norm_ffn_g'][1], full['up_t1'], full['ffn_conv'][1], full['down1'],
                                    sv_ffn1, "l1_ffn", 1, riders)
    dx4, dmem1, dg_xa1 = _xa_bwd(dx5, x4, inp['norm_xa_g'][1], mem_n, full['wq1'], full['wkv_t1'], full['wo1'],
                                 sv_xa1, "l1_xa", 1, riders)
    dxn1, dpool_w, dpool_scale = _pool_bwd(xn1, full['pool_w'], full['pool_scale'], dx4, "l1_pool_bwd")
    pool_pieces = jnp.swapaxes(dpool_w.astype(BF16).reshape(len(POOL_WINDOWS), N_DEV, -1, POOL_GROUP), 0, 1)
    riders.grad('pool_w', pool_pieces.reshape(N_DEV, -1, PACK_COLS))
    dx3, dg_mix1 = riders.run("l1_mix_norm_bwd", _rms_bwd, x3, inp['norm_mix_g'][1], dxn1, dx4)
    dx2, dconv0, dg_ffn0 = _ffn_bwd(dx3, x2, inp['norm_ffn_g'][0], full['up_t0'], full['ffn_conv'][0], full['down0'],
                                    sv_ffn0, "l0_ffn", 0, riders)
    dx1, dmem0, dg_xa0 = _xa_bwd(dx2, x1, inp['norm_xa_g'][0], mem_n, full['wq0'], full['wkv_t0'], full['wo0'],
                                 sv_xa0, "l0_xa", 0, riders)
    dxn0, g_mix0 = _hybrid_bwd(dx1, xn0, wts0, p0, sv_mix0, riders)
    grad_x, dg_mix0 = _rms_bwd(x0, inp['norm_mix_g'][0], dxn0, dx1, "l0_mix_norm_bwd")
    _, dg_mem = _rms_bwd(inp['mem'][0], inp['norm_mem_g'], dmem0 + dmem1, None, "mem_norm_bwd")
    assert not riders.waiting and not riders.groups, (list(riders.waiting), riders.groups)

    db_in, dc_out, da_row, dd = g_mix0['s5']
    dlb_re, dlb_im, dbb_re, dbb_im, dc_re, dc_im = _s5_matrix_grads(db_in, dc_out, da_row)
    dlam_re, dlam_im, dbr, dbi, dlog_dt = disc_vjp((dlb_re, dlb_im, dbb_re, dbb_im))

    rep_grads = {
        'norm_mix_g': jnp.concatenate([dg_mix0, dg_mix1], 0), 'norm_xa_g': jnp.concatenate([dg_xa0, dg_xa1], 0),
        'norm_ffn_g': jnp.concatenate([dg_ffn0, dg_ffn1], 0), 'norm_mem_g': dg_mem.reshape(-1),
        'norm_final_g': dg_final.reshape(-1), 'a_log_a': g_mix0['a_log_a'], 'dt_bias_a': g_mix0['dt_bias_a'],
        'onorm_g_a': g_mix0['onorm_g_a'], 'ssm_lambda_re': dlam_re[None], 'ssm_lambda_im': dlam_im[None],
        'ssm_b_re': dbr[None], 'ssm_b_im': dbi[None], 'ssm_c_re': dc_re[None], 'ssm_c_im': dc_im[None],
        'ssm_d': dd.reshape(1, N_GROUPS, SSM_GROUP), 'ssm_log_dt': dlog_dt[None], 'b_glu_b': g_mix0['b_glu_b']}
    small_grads = {'conv_qkv_a': g_mix0['conv_qkv_a'][None], 'pool_scale': dpool_scale,
                   'ffn_conv': jnp.stack([dconv0, dconv1])}
    for key, rows in PIECES:
        assert riders.reduced[key].shape == (rows, PACK_COLS), key
    return loss_part, grad_x, riders.reduced, rep_grads, small_grads


def _reduce_small_gradients(rep_grads, small_grads):
    misc_local = _pack([rep_grads[n] for n in REP_NAMES] + [small_grads[n] for n in SMALL_NAMES])
    (misc_all,) = _comm_only(_gather_comm([misc_local]), "gather_small_grads")
    return _sum_leading(misc_all, "small_grads_sum")


def _update(inp, loss_part, grad_x, big_reduced, misc_sum):
    dev = _device_index()
    piece = lambda key, rows=None: big_reduced[key] if rows is None else big_reduced[key][:rows]
    both = lambda name, fn: jnp.stack([fn(piece(name + '0')), fn(piece(name + '1'))])
    ident, transpose = (lambda a: a), (lambda a: a.T)
    grads = {'w_in_ab': piece('w_in_t', W_IN_PIECE).T[None], 'w_glu_b': piece('w_glu').reshape(inp['w_glu_b'].shape),
             'w_out_ab': piece('w_out')[None], 'pool_w': piece('pool_w').reshape(inp['pool_w'].shape),
             'xa_wq': both('wq', ident), 'xa_wkv': both('wkv_t', transpose), 'xa_wo': both('wo', ident),
             'ffn_w_up': both('up_t', transpose), 'ffn_w_down': both('down', ident)}
    misc = _unpack(misc_sum, [inp[n].shape for n in REP_NAMES] + [SMALL_SHARDED[n][0] for n in SMALL_NAMES])
    for n, g in zip(REP_NAMES, misc):
        grads[n] = g
    for n, g in zip(SMALL_NAMES, misc[len(REP_NAMES):]):
        grads[n] = lax.dynamic_index_in_dim(_split_shards(g, SMALL_SHARDED[n][1]), dev, 0, keepdims=False
                                            ).reshape(inp[n].shape)
    upd = {}
    for n in BIG_NAMES:
        upd[n] = _adamw(inp[n], grads[n], inp['m_' + n], inp['v_' + n], "adamw_" + n)
    tiny_names = REP_NAMES + SMALL_NAMES
    rep_total = sum(int(np.prod(inp[n].shape)) for n in REP_NAMES)
    packs = [_pack([inp[prefix + n] for n in tiny_names]) for prefix in ('', 'm_', 'v_')]
    g_pack = _pack([misc_sum.reshape(-1)[:rep_total]] + [grads[n] for n in SMALL_NAMES])
    tiny_out = [_unpack(o, [inp[n].shape for n in tiny_names])
                for o in _adamw(packs[0], g_pack, packs[1], packs[2], "adamw_small")]
    for i, n in enumerate(tiny_names):
        upd[n] = tuple(o[i] for o in tiny_out)

    loss = lax.psum(loss_part[0, 0], ("x", "y", "c"))
    outs = [loss, grad_x[None]]
    outs += [grads[n] for n in WEIGHT_NAMES]
    for i in range(3):
        outs += [upd[n][i] for n in WEIGHT_NAMES]
    return tuple(outs)


def _step(inp):
    loss_part, grad_x, big_reduced, rep_grads, small_grads = _local_step(inp)
    misc_sum = _reduce_small_gradients(rep_grads, small_grads)
    return _update(inp, loss_part, grad_x, big_reduced, misc_sum)


INPUT_NAMES = (['x', 'mem'] + WEIGHT_NAMES + ['loss_target'] + ['m_' + n for n in WEIGHT_NAMES]
               + ['v_' + n for n in WEIGHT_NAMES])


def kernel(x, mem, norm_mix_g, norm_xa_g, norm_ffn_g, norm_mem_g, norm_final_g, w_in_ab, conv_qkv_a, a_log_a, dt_bias_a, onorm_g_a, ssm_lambda_re, ssm_lambda_im, ssm_b_re, ssm_b_im, ssm_c_re, ssm_c_im, ssm_d, ssm_log_dt, w_glu_b, b_glu_b, w_out_ab, pool_w, pool_scale, xa_wq, xa_wkv, xa_wo, ffn_w_up, ffn_conv, ffn_w_down, loss_target, m_norm_mix_g, m_norm_xa_g, m_norm_ffn_g, m_norm_mem_g, m_norm_final_g, m_w_in_ab, m_conv_qkv_a, m_a_log_a, m_dt_bias_a, m_onorm_g_a, m_ssm_lambda_re, m_ssm_lambda_im, m_ssm_b_re, m_ssm_b_im, m_ssm_c_re, m_ssm_c_im, m_ssm_d, m_ssm_log_dt, m_w_glu_b, m_b_glu_b, m_w_out_ab, m_pool_w, m_pool_scale, m_xa_wq, m_xa_wkv, m_xa_wo, m_ffn_w_up, m_ffn_conv, m_ffn_w_down, v_norm_mix_g, v_norm_xa_g, v_norm_ffn_g, v_norm_mem_g, v_norm_final_g, v_w_in_ab, v_conv_qkv_a, v_a_log_a, v_dt_bias_a, v_onorm_g_a, v_ssm_lambda_re, v_ssm_lambda_im, v_ssm_b_re, v_ssm_b_im, v_ssm_c_re, v_ssm_c_im, v_ssm_d, v_ssm_log_dt, v_w_glu_b, v_b_glu_b, v_w_out_ab, v_pool_w, v_pool_scale, v_xa_wq, v_xa_wkv, v_xa_wo, v_ffn_w_up, v_ffn_conv, v_ffn_w_down):
    args = (x, mem, norm_mix_g, norm_xa_g, norm_ffn_g, norm_mem_g, norm_final_g, w_in_ab, conv_qkv_a, a_log_a, dt_bias_a, onorm_g_a, ssm_lambda_re, ssm_lambda_im, ssm_b_re, ssm_b_im, ssm_c_re, ssm_c_im, ssm_d, ssm_log_dt, w_glu_b, b_glu_b, w_out_ab, pool_w, pool_scale, xa_wq, xa_wkv, xa_wo, ffn_w_up, ffn_conv, ffn_w_down, loss_target, m_norm_mix_g, m_norm_xa_g, m_norm_ffn_g, m_norm_mem_g, m_norm_final_g, m_w_in_ab, m_conv_qkv_a, m_a_log_a, m_dt_bias_a, m_onorm_g_a, m_ssm_lambda_re, m_ssm_lambda_im, m_ssm_b_re, m_ssm_b_im, m_ssm_c_re, m_ssm_c_im, m_ssm_d, m_ssm_log_dt, m_w_glu_b, m_b_glu_b, m_w_out_ab, m_pool_w, m_pool_scale, m_xa_wq, m_xa_wkv, m_xa_wo, m_ffn_w_up, m_ffn_conv, m_ffn_w_down, v_norm_mix_g, v_norm_xa_g, v_norm_ffn_g, v_norm_mem_g, v_norm_final_g, v_w_in_ab, v_conv_qkv_a, v_a_log_a, v_dt_bias_a, v_onorm_g_a, v_ssm_lambda_re, v_ssm_lambda_im, v_ssm_b_re, v_ssm_b_im, v_ssm_c_re, v_ssm_c_im, v_ssm_d, v_ssm_log_dt, v_w_glu_b, v_b_glu_b, v_w_out_ab, v_pool_w, v_pool_scale, v_xa_wq, v_xa_wkv, v_xa_wo, v_ffn_w_up, v_ffn_conv, v_ffn_w_down)
    return _step(dict(zip(INPUT_NAMES, args)))
```

```python
import functools
import math

import numpy as np
import jax
import jax.numpy as jnp
from jax import lax
from jax.experimental import pallas as pl
from jax.experimental.pallas import tpu as pltpu

F32, BF16 = jnp.float32, jnp.bfloat16
HIGH, HIGHEST = lax.Precision.HIGH, lax.Precision.HIGHEST
MESH = pl.DeviceIdType.MESH

N_DEV = 8
SEQ, D_MODEL, MEM_LEN = 2048, 1024, 256
WIDTH_A, N_HEADS_A, HEAD_A, CONV_A = 512, 4, 128, 4
GDR_CHUNK = 128
GDR_HEADS_PER_STEP = 4
SSM_WIDTH, SSM_GROUP, N_GROUPS, SSM_STATE = 512, 16, 32, 64
SSM_CH = N_GROUPS * SSM_STATE
SCAN_CB = 512
POOL_WINDOWS = (2, 4, 8, 16)
POOL_GROUP = 256
N_HEADS_X, HEAD_X = 4, 256
D_FF, CONV_FFN = 2816, 3
RMS_EPS = 1e-6
ADAM_LR, ADAM_B1, ADAM_B2, ADAM_EPS, ADAM_WD, ADAM_STEP = 0.001, 0.9, 0.999, 1e-08, 0.01, 10
LANE = 128
PACK_COLS = 1024
VMEM_LIMIT_BYTES = 56 * 1024 * 1024


def _params(sem=None):
    return pltpu.CompilerParams(dimension_semantics=sem, vmem_limit_bytes=VMEM_LIMIT_BYTES)


class Comm:
    def __init__(self, inputs, out_shapes, sems, start, end, mid=None):
        self.inputs, self.out_shapes, self.sems = list(inputs), list(out_shapes), list(sems)
        self.start, self.mid, self.end = start, mid, end


def _merge_comms(comms):
    comms = [c for c in comms if c is not None]
    if not comms:
        return None, []
    bounds, ni, no, ns = [], 0, 0, 0
    for c in comms:
        bounds.append((ni, no, ns))
        ni, no, ns = ni + len(c.inputs), no + len(c.out_shapes), ns + len(c.sems)

    def phase(which):
        def run(ins, outs, sems):
            for c, (i0, o0, s0) in zip(comms, bounds):
                fn = getattr(c, which)
                if fn is not None:
                    fn(ins[i0:i0 + len(c.inputs)], outs[o0:o0 + len(c.out_shapes)], sems[s0:s0 + len(c.sems)])
        return run

    merged = Comm([a for c in comms for a in c.inputs], [s for c in comms for s in c.out_shapes],
                  [s for c in comms for s in c.sems], phase("start"), phase("end"), phase("mid"))
    return merged, [(o0, o0 + len(c.out_shapes)) for c, (_, o0, _) in zip(comms, bounds)]


def _call(body, *, name, grid, in_specs, out_specs, out_shape, args, scratch_shapes=(), sem=None, comm=None):
    single = not isinstance(out_shape, (list, tuple))
    out_specs_l = [out_specs] if single else list(out_specs)
    out_shape_l = [out_shape] if single else list(out_shape)
    scratch_shapes = list(scratch_shapes)
    merged, spans = _merge_comms(comm if isinstance(comm, (list, tuple)) else [comm])
    if merged is None:
        outs = pl.pallas_call(body, name=name, grid=grid, in_specs=list(in_specs), out_specs=out_specs_l,
                              out_shape=out_shape_l, scratch_shapes=scratch_shapes, compiler_params=_params(sem))(*args)
        outs = outs[0] if single else outs
        return outs if comm is None else (outs, [])
    n_in, n_out, n_scr = len(in_specs), len(out_specs_l), len(scratch_shapes)
    ci, co = len(merged.inputs), len(merged.out_shapes)
    total = int(np.prod(grid))

    def wrapped(*refs):
        ins, cins = refs[:n_in], refs[n_in:n_in + ci]
        outs, couts = refs[n_in + ci:n_in + ci + n_out], refs[n_in + ci + n_out:n_in + ci + n_out + co]
        scr, csems = refs[n_in + ci + n_out + co:n_in + ci + n_out + co + n_scr], refs[n_in + ci + n_out + co + n_scr:]
        lin = pl.program_id(0)
        for d in range(1, len(grid)):
            lin = lin * grid[d] + pl.program_id(d)
        pl.when(lin == 0)(lambda: merged.start(cins, couts, csems))
        body(*ins, *outs, *scr)
        def finish():
            merged.mid(cins, couts, csems)
            merged.end(cins, couts, csems)

        pl.when(lin == total - 1)(finish)

    any_spec = pl.BlockSpec(memory_space=pl.ANY)
    res = pl.pallas_call(
        wrapped, name=name, grid=grid, in_specs=list(in_specs) + [any_spec] * ci,
        out_specs=out_specs_l + [any_spec] * co, out_shape=out_shape_l + merged.out_shapes,
        scratch_shapes=scratch_shapes + merged.sems,
        compiler_params=_params(("arbitrary",) * len(grid)))(*args, *merged.inputs)
    outs, couts = res[:n_out], res[n_out:]
    return (outs[0] if single else list(outs)), [list(couts[a:b]) for a, b in spans]


def _comm_only(comm, name):
    def body():
        pass

    _, couts = _call(body, name=name, grid=(1,), in_specs=[], out_specs=[], out_shape=[], args=[], comm=comm)
    return couts[0]


def _tile(dim, pref):
    best = None
    for t in range(LANE, min(dim, pref) + 1, LANE):
        if dim % t == 0:
            best = t
    return best if best is not None else dim


MM_VMEM_BUDGET = 40 * 1024 * 1024


def _mm_tiles(m, n, k, a_bytes, b_bytes, o_bytes, r_bytes):
    for tk in (k, _tile(k, 2048), _tile(k, 1024), _tile(k, 512)):
        for tm, tn in ((1024, 1536), (1024, 1024), (1024, 512), (512, 512), (256, 512), (256, 256)):
            tm, tn = _tile(m, tm), _tile(n, tn)
            acc = 0 if tk == k else tm * tn * 4
            need = 2 * (tm * tk * a_bytes + tk * tn * b_bytes + tm * tn * (o_bytes + r_bytes)) + acc
            if need <= MM_VMEM_BUDGET:
                return tm, tn, tk
    raise ValueError("no matmul tiling fits VMEM")


def _mm(a, b, mode, name, out_dtype=F32, res=None, comm=None):
    if mode == "nn":
        (m, k), n = a.shape, b.shape[1]
    elif mode == "nt":
        (m, k), n = a.shape, b.shape[0]
    else:
        (k, m), n = a.shape, b.shape[1]
    tm, tn, tk = _mm_tiles(m, n, k, a.dtype.itemsize, b.dtype.itemsize, jnp.dtype(out_dtype).itemsize,
                           0 if res is None else res.dtype.itemsize)
    nk = k // tk
    dims = {"nn": ((1,), (0,)), "nt": ((1,), (1,)), "tn": ((0,), (0,))}[mode]

    def body(*refs):
        if res is None:
            a_ref, b_ref, o_ref = refs[:3]
            r_ref = None
        else:
            a_ref, b_ref, r_ref, o_ref = refs[:4]
        part = lax.dot_general(a_ref[...].astype(BF16), b_ref[...].astype(BF16), (dims, ((), ())),
                               preferred_element_type=F32)

        def finish(out):
            if r_ref is not None:
                out = out + r_ref[...].astype(F32)
            o_ref[...] = out.astype(out_dtype)

        if nk == 1:
            finish(part)
            return
        acc = refs[-1]
        kk = pl.program_id(2)

        @pl.when(kk == 0)
        def _():
            acc[...] = part

        @pl.when(kk > 0)
        def _():
            acc[...] += part

        @pl.when(kk == nk - 1)
        def _():
            finish(acc[...])

    a_spec = (pl.BlockSpec((tk, tm), lambda i, j, q: (q, i)) if mode == "tn"
              else pl.BlockSpec((tm, tk), lambda i, j, q: (i, q)))
    b_spec = (pl.BlockSpec((tn, tk), lambda i, j, q: (j, q)) if mode == "nt"
              else pl.BlockSpec((tk, tn), lambda i, j, q: (q, j)))
    o_spec = pl.BlockSpec((tm, tn), lambda i, j, q: (i, j))
    in_specs, args = [a_spec, b_spec], [a, b]
    if res is not None:
        in_specs.append(o_spec)
        args.append(res)
    return _call(body, name=name, grid=(m // tm, n // tn, nk), in_specs=in_specs, out_specs=o_spec,
                 out_shape=jax.ShapeDtypeStruct((m, n), out_dtype),
                 scratch_shapes=[] if nk == 1 else [pltpu.VMEM((tm, tn), F32)],
                 sem=("parallel", "parallel", "arbitrary"), args=args, comm=comm)


def _mm_bd(a, b, mode, name, out_dtype=F32, res=None, comm=None, tm=1024):
    if mode == "tn":
        k = a.shape[0]
        nb = min(a.shape[1], b.shape[1]) // LANE
        ma, n = a.shape[1] // nb, b.shape[1] // nb

        def body(a_ref, b_ref, o_ref):
            o_ref[0] = lax.dot_general(a_ref[...].astype(BF16), b_ref[...].astype(BF16), (((0,), (0,)), ((), ())),
                                       preferred_element_type=F32).astype(out_dtype)

        return _call(body, name=name, grid=(nb,),
                     in_specs=[pl.BlockSpec((k, ma), lambda j: (0, j)), pl.BlockSpec((k, n), lambda j: (0, j))],
                     out_specs=pl.BlockSpec((1, ma, n), lambda j: (j, 0, 0)),
                     out_shape=jax.ShapeDtypeStruct((nb, ma, n), out_dtype), sem=("parallel",), args=(a, b), comm=comm)
    m = a.shape[0]
    nb = b.shape[0]
    ka = a.shape[1] // nb
    n = b.shape[2] if mode == "nn" else b.shape[1]
    tm = _tile(m, tm)
    dims = ((1,), (0,)) if mode == "nn" else ((1,), (1,))

    def body(*refs):
        if res is None:
            a_ref, b_ref, o_ref = refs
            r_ref = None
        else:
            a_ref, b_ref, r_ref, o_ref = refs
        out = lax.dot_general(a_ref[...].astype(BF16), b_ref[0].astype(BF16), (dims, ((), ())),
                              preferred_element_type=F32)
        if r_ref is not None:
            out = out + r_ref[...].astype(F32)
        o_ref[...] = out.astype(out_dtype)

    o_spec = pl.BlockSpec((tm, n), lambda i, j: (i, j))
    in_specs = [pl.BlockSpec((tm, ka), lambda i, j: (i, j)), pl.BlockSpec((1,) + b.shape[1:], lambda i, j: (j, 0, 0))]
    args = [a, b]
    if res is not None:
        in_specs.append(o_spec)
        args.append(res)
    return _call(body, name=name, grid=(m // tm, nb), in_specs=in_specs, out_specs=o_spec,
                 out_shape=jax.ShapeDtypeStruct((m, nb * n), out_dtype), sem=("parallel", "parallel"),
                 args=args, comm=comm)


def _rms_fwd(x, g, out_dtype, name, tr=256):
    rows, d = x.shape

    def body(x_ref, g_ref, o_ref):
        xv = x_ref[...]
        r = lax.rsqrt(jnp.mean(xv * xv, axis=-1, keepdims=True) + RMS_EPS)
        o_ref[...] = (xv * r * g_ref[...]).astype(out_dtype)

    return pl.pallas_call(
        body, name=name, grid=(rows // tr,),
        in_specs=[pl.BlockSpec((tr, d), lambda i: (i, 0)), pl.BlockSpec((1, d), lambda i: (0, 0))],
        out_specs=pl.BlockSpec((tr, d), lambda i: (i, 0)), out_shape=jax.ShapeDtypeStruct((rows, d), out_dtype),
        compiler_params=_params(("parallel",)))(x, g.reshape(1, d))


def _rms_bwd(x, g, dy, dres, name, tr=256, comm=None):
    rows, d = x.shape

    def body(*refs):
        if dres is None:
            x_ref, g_ref, dy_ref, dx_ref, dg_ref = refs
            r_ref = None
        else:
            x_ref, g_ref, dy_ref, r_ref, dx_ref, dg_ref = refs

        @pl.when(pl.program_id(0) == 0)
        def _():
            dg_ref[...] = jnp.zeros_like(dg_ref)

        xv, dyv = x_ref[...], dy_ref[...].astype(F32)
        r = lax.rsqrt(jnp.mean(xv * xv, axis=-1, keepdims=True) + RMS_EPS)
        xh = xv * r
        dyg = dyv * g_ref[...]
        dx = r * (dyg - xh * jnp.mean(dyg * xh, axis=-1, keepdims=True))
        if r_ref is not None:
            dx = dx + r_ref[...]
        dx_ref[...] = dx
        dg_ref[...] += jnp.sum(dyv * xh, axis=0, keepdims=True)

    blk = pl.BlockSpec((tr, d), lambda i: (i, 0))
    vec = pl.BlockSpec((1, d), lambda i: (0, 0))
    in_specs, args = [blk, vec, blk], [x, g.reshape(1, d), dy]
    if dres is not None:
        in_specs.append(blk)
        args.append(dres)
    return _call(
        body, name=name, grid=(rows // tr,), in_specs=in_specs, out_specs=[blk, vec],
        out_shape=[jax.ShapeDtypeStruct((rows, d), F32), jax.ShapeDtypeStruct((1, d), F32)],
        sem=("arbitrary",), args=args, comm=comm)


def _loss_head(x, g, target, name, tr=256):
    rows, d = x.shape

    def body(x_ref, g_ref, t_ref, loss_ref, dx_ref, dg_ref):
        @pl.when(pl.program_id(0) == 0)
        def _():
            dg_ref[...] = jnp.zeros_like(dg_ref)
            loss_ref[...] = jnp.zeros_like(loss_ref)

        xv = x_ref[...]
        r = lax.rsqrt(jnp.mean(xv * xv, axis=-1, keepdims=True) + RMS_EPS)
        xh = xv * r
        err = xh * g_ref[...] - t_ref[...]
        loss_ref[...] += 0.5 * jnp.sum(jnp.mean(err * err, axis=-1, keepdims=True), keepdims=True)
        dyv = err * (1.0 / d)
        dyg = dyv * g_ref[...]
        dx_ref[...] = r * (dyg - xh * jnp.mean(dyg * xh, axis=-1, keepdims=True))
        dg_ref[...] += jnp.sum(dyv * xh, axis=0, keepdims=True)

    blk = pl.BlockSpec((tr, d), lambda i: (i, 0))
    vec = pl.BlockSpec((1, d), lambda i: (0, 0))
    return pl.pallas_call(
        body, name=name, grid=(rows // tr,), in_specs=[blk, vec, blk],
        out_specs=[pl.BlockSpec((1, 1), lambda i: (0, 0)), blk, vec],
        out_shape=[jax.ShapeDtypeStruct((1, 1), F32), jax.ShapeDtypeStruct((rows, d), F32),
                   jax.ShapeDtypeStruct((1, d), F32)],
        compiler_params=_params(("arbitrary",)))(x, g.reshape(1, d), target)


def _shift_down(x, s):
    rows = lax.broadcasted_iota(jnp.int32, x.shape, 0)
    return jnp.where(rows >= s, pltpu.roll(x, s, 0), 0.0)


def _shift_up(x, s):
    n = x.shape[0]
    rows = lax.broadcasted_iota(jnp.int32, x.shape, 0)
    return jnp.where(rows < n - s, pltpu.roll(x, n - s, 0), 0.0)


def _sigmoid(x):
    return 1.0 / (1.0 + jnp.exp(-x))


def _silu_and_grad(x):
    s = _sigmoid(x)
    return x * s, s * (1.0 + x * (1.0 - s))


_GELU_C0, _GELU_C1 = math.sqrt(2.0 / math.pi), 0.044715


def _gelu_and_grad(x):
    th = jnp.tanh(_GELU_C0 * (x + _GELU_C1 * x * x * x))
    y = 0.5 * x * (1.0 + th)
    dy = 0.5 * (1.0 + th) + 0.5 * x * (1.0 - th * th) * _GELU_C0 * (1.0 + 3.0 * _GELU_C1 * x * x)
    return y, dy


def _ffn_act_fwd(h, w, name, tc=256, comm=None):
    t = h.shape[0]
    nb = D_FF // tc

    def body(hg_ref, hv_ref, wg_ref, wv_ref, a_ref):
        def conv(x, wr):
            return wr[2:3, :] * x + wr[1:2, :] * _shift_down(x, 1) + wr[0:1, :] * _shift_down(x, 2)

        cg = conv(hg_ref[...], wg_ref[...])
        cv = conv(hv_ref[...], wv_ref[...])
        a_ref[...] = (cg * _sigmoid(cg) * cv).astype(BF16)

    return _call(
        body, name=name, grid=(nb,),
        in_specs=[pl.BlockSpec((t, tc), lambda j: (0, j)), pl.BlockSpec((t, tc), lambda j: (0, j + nb)),
                  pl.BlockSpec((CONV_FFN, tc), lambda j: (0, j)), pl.BlockSpec((CONV_FFN, tc), lambda j: (0, j + nb))],
        out_specs=pl.BlockSpec((t, tc), lambda j: (0, j)), out_shape=jax.ShapeDtypeStruct((t, D_FF), BF16),
        sem=("parallel",), args=(h, h, w, w), comm=comm)


def _ffn_act_bwd(h, w, da, name, tc=256, comm=None):
    t = h.shape[0]
    nb = D_FF // tc

    def body(hg_ref, hv_ref, wg_ref, wv_ref, da_ref, dhg_ref, dhv_ref, dwg_ref, dwv_ref):
        hg, hv, wg, wv = hg_ref[...], hv_ref[...], wg_ref[...], wv_ref[...]
        hg1, hg2, hv1, hv2 = _shift_down(hg, 1), _shift_down(hg, 2), _shift_down(hv, 1), _shift_down(hv, 2)
        cg = wg[2:3, :] * hg + wg[1:2, :] * hg1 + wg[0:1, :] * hg2
        cv = wv[2:3, :] * hv + wv[1:2, :] * hv1 + wv[0:1, :] * hv2
        sg, dsg = _silu_and_grad(cg)
        dav = da_ref[...].astype(F32)
        dcv = dav * sg
        dcg = dav * cv * dsg

        def conv_t(dc, wr):
            return wr[2:3, :] * dc + wr[1:2, :] * _shift_up(dc, 1) + wr[0:1, :] * _shift_up(dc, 2)

        dhg_ref[...] = conv_t(dcg, wg).astype(BF16)
        dhv_ref[...] = conv_t(dcv, wv).astype(BF16)
        dwg_ref[0:1, :] = jnp.sum(dcg * hg2, axis=0, keepdims=True)
        dwg_ref[1:2, :] = jnp.sum(dcg * hg1, axis=0, keepdims=True)
        dwg_ref[2:3, :] = jnp.sum(dcg * hg, axis=0, keepdims=True)
        dwv_ref[0:1, :] = jnp.sum(dcv * hv2, axis=0, keepdims=True)
        dwv_ref[1:2, :] = jnp.sum(dcv * hv1, axis=0, keepdims=True)
        dwv_ref[2:3, :] = jnp.sum(dcv * hv, axis=0, keepdims=True)

    big = lambda off: pl.BlockSpec((t, tc), lambda j: (0, j + off))
    small = lambda off: pl.BlockSpec((CONV_FFN, tc), lambda j: (0, j + off))
    res = _call(
        body, name=name, grid=(nb,),
        in_specs=[big(0), big(nb), small(0), small(nb), big(0)],
        out_specs=[big(0), big(0), small(0), small(0)],
        out_shape=[jax.ShapeDtypeStruct((t, D_FF), BF16), jax.ShapeDtypeStruct((t, D_FF), BF16),
                   jax.ShapeDtypeStruct((CONV_FFN, D_FF), F32), jax.ShapeDtypeStruct((CONV_FFN, D_FF), F32)],
        sem=("parallel",), args=(h, h, w, w, da), comm=comm)
    (dhg, dhv, dwg, dwv), couts = res if comm is not None else (res, None)
    out = (jnp.concatenate([dhg, dhv], axis=1), jnp.concatenate([dwg, dwv], axis=1))
    return out if comm is None else (out, couts)


def _attn_probs(q, k):
    s = lax.dot_general(q.astype(BF16), k.astype(BF16), (((1,), (1,)), ((), ())),
                        preferred_element_type=F32) * (HEAD_X ** -0.5)
    s = s - jnp.max(s, axis=-1, keepdims=True)
    p = jnp.exp(s)
    return p / jnp.sum(p, axis=-1, keepdims=True)


def _attn_fwd(q, kv, name, tq=512, comm=None):
    t = q.shape[0]

    def body(q_ref, k_ref, v_ref, o_ref):
        p = _attn_probs(q_ref[...], k_ref[...])
        o_ref[...] = jnp.dot(p.astype(BF16), v_ref[...].astype(BF16), preferred_element_type=F32).astype(BF16)

    return _call(
        body, name=name, grid=(N_HEADS_X, t // tq),
        in_specs=[pl.BlockSpec((tq, HEAD_X), lambda h, i: (i, h)),
                  pl.BlockSpec((MEM_LEN, HEAD_X), lambda h, i: (0, h)),
                  pl.BlockSpec((MEM_LEN, HEAD_X), lambda h, i: (0, h + N_HEADS_X))],
        out_specs=pl.BlockSpec((tq, HEAD_X), lambda h, i: (i, h)),
        out_shape=jax.ShapeDtypeStruct((t, N_HEADS_X * HEAD_X), BF16),
        sem=("parallel", "parallel"), args=(q, kv, kv), comm=comm)


def _attn_bwd(q, kv, do, name, tq=512):
    t = q.shape[0]

    def body(q_ref, k_ref, v_ref, do_ref, dq_ref, dk_ref, dv_ref):
        @pl.when(pl.program_id(1) == 0)
        def _():
            dk_ref[...] = jnp.zeros_like(dk_ref)
            dv_ref[...] = jnp.zeros_like(dv_ref)

        qb, kb, vb, dob = (r[...].astype(BF16) for r in (q_ref, k_ref, v_ref, do_ref))
        p = _attn_probs(qb, kb)
        dp = lax.dot_general(dob, vb, (((1,), (1,)), ((), ())), preferred_element_type=F32)
        ds = p * (dp - jnp.sum(dp * p, axis=-1, keepdims=True)) * (HEAD_X ** -0.5)
        dsb = ds.astype(BF16)
        dq_ref[...] = jnp.dot(dsb, kb, preferred_element_type=F32).astype(BF16)
        dk_ref[...] += lax.dot_general(dsb, qb, (((0,), (0,)), ((), ())), preferred_element_type=F32)
        dv_ref[...] += lax.dot_general(p.astype(BF16), dob, (((0,), (0,)), ((), ())), preferred_element_type=F32)

    qs = pl.BlockSpec((tq, HEAD_X), lambda h, i: (i, h))
    ms = pl.BlockSpec((MEM_LEN, HEAD_X), lambda h, i: (0, h))
    return pl.pallas_call(
        body, name=name, grid=(N_HEADS_X, t // tq),
        in_specs=[qs, ms, pl.BlockSpec((MEM_LEN, HEAD_X), lambda h, i: (0, h + N_HEADS_X)), qs],
        out_specs=[qs, ms, ms],
        out_shape=[jax.ShapeDtypeStruct((t, D_MODEL), BF16), jax.ShapeDtypeStruct((MEM_LEN, D_MODEL), F32),
                   jax.ShapeDtypeStruct((MEM_LEN, D_MODEL), F32)],
        compiler_params=_params(("parallel", "arbitrary")))(q, kv, kv, do)


def _pool_counts(t, win):
    pos = lax.broadcasted_iota(jnp.int32, (t, 1), 0).astype(F32) + 1.0
    return 1.0 / jnp.minimum(pos, float(win))


def _pool_delta(xv, win):
    s, step = xv, 1
    while step < win:
        s = s + _shift_down(s, step)
        step *= 2
    return s * _pool_counts(xv.shape[0], win) - xv


def _pool_delta_t(dv, win):
    s, step = dv * _pool_counts(dv.shape[0], win), 1
    while step < win:
        s = s + _shift_up(s, step)
        step *= 2
    return s - dv


def _pool_fwd(xn, w, scale, res, name):
    t = xn.shape[0]

    def make_branch(win, xn_ref, w_ref, s_ref, r_ref, o_ref):
        def branch():
            dl = _pool_delta(xn_ref[...], win)
            y = jnp.dot(dl.astype(BF16), w_ref[0], preferred_element_type=F32)
            o_ref[...] = r_ref[...] + y * s_ref[...]
        return branch

    def body(xn_ref, w_ref, s_ref, r_ref, o_ref):
        for gi, win in enumerate(POOL_WINDOWS):
            pl.when(pl.program_id(0) == gi)(make_branch(win, xn_ref, w_ref, s_ref, r_ref, o_ref))

    blk = pl.BlockSpec((t, POOL_GROUP), lambda g: (0, g))
    return pl.pallas_call(
        body, name=name, grid=(len(POOL_WINDOWS),),
        in_specs=[blk, pl.BlockSpec((1, POOL_GROUP, POOL_GROUP), lambda g: (g, 0, 0)),
                  pl.BlockSpec((1, POOL_GROUP), lambda g: (0, g)), blk],
        out_specs=blk, out_shape=jax.ShapeDtypeStruct((t, D_MODEL), F32),
        compiler_params=_params(("parallel",)))(xn, w, scale, res)


def _pool_bwd(xn, w, scale, dmix, name):
    t = xn.shape[0]

    def make_branch(win, xn_ref, w_ref, s_ref, d_ref, dxn_ref, dw_ref, ds_ref):
        def branch():
            dl = _pool_delta(xn_ref[...], win).astype(BF16)
            wv = w_ref[0]
            dm = d_ref[...]
            y = jnp.dot(dl, wv, preferred_element_type=F32)
            ds_ref[...] = jnp.sum(dm * y, axis=0, keepdims=True)
            dy = (dm * s_ref[...]).astype(BF16)
            dw_ref[0] = lax.dot_general(dl, dy, (((0,), (0,)), ((), ())), preferred_element_type=F32)
            ddl = lax.dot_general(dy, wv, (((1,), (1,)), ((), ())), preferred_element_type=F32)
            dxn_ref[...] = _pool_delta_t(ddl, win)
        return branch

    def body(*refs):
        for gi, win in enumerate(POOL_WINDOWS):
            pl.when(pl.program_id(0) == gi)(make_branch(win, *refs))

    blk = pl.BlockSpec((t, POOL_GROUP), lambda g: (0, g))
    wspec = pl.BlockSpec((1, POOL_GROUP, POOL_GROUP), lambda g: (g, 0, 0))
    vec = pl.BlockSpec((1, POOL_GROUP), lambda g: (0, g))
    return pl.pallas_call(
        body, name=name, grid=(len(POOL_WINDOWS),), in_specs=[blk, wspec, vec, blk], out_specs=[blk, wspec, vec],
        out_shape=[jax.ShapeDtypeStruct((t, D_MODEL), F32),
                   jax.ShapeDtypeStruct((len(POOL_WINDOWS), POOL_GROUP, POOL_GROUP), F32),
                   jax.ShapeDtypeStruct((1, D_MODEL), F32)],
        compiler_params=_params(("parallel",)))(xn, w, scale, dmix)


def _qkv_conv(h, wr):
    return (wr[3:4, :] * h + wr[2:3, :] * _shift_down(h, 1) + wr[1:2, :] * _shift_down(h, 2)
            + wr[0:1, :] * _shift_down(h, 3))


def _qkv_pre_fwd(h, w, col0, ncols, normalize, scale, name):
    t = h.shape[0]

    def body(h_ref, w_ref, o_ref):
        c = _qkv_conv(h_ref[...], w_ref[...])
        s = c * _sigmoid(c)
        if normalize:
            s = s * lax.rsqrt(jnp.sum(s * s, axis=-1, keepdims=True) + 1e-6) * scale
        o_ref[...] = s

    return pl.pallas_call(
        body, name=name, grid=(ncols,),
        in_specs=[pl.BlockSpec((t, HEAD_A), lambda j: (0, j + col0)), pl.BlockSpec((CONV_A, HEAD_A), lambda j: (0, j + col0))],
        out_specs=pl.BlockSpec((t, HEAD_A), lambda j: (0, j)), out_shape=jax.ShapeDtypeStruct((t, ncols * HEAD_A), F32),
        compiler_params=_params(("parallel",)))(h, w)


def _qkv_pre_bwd(h, w, dy, col0, ncols, normalize, scale, name):
    t = h.shape[0]

    def body(h_ref, w_ref, dy_ref, dh_ref, dw_ref):
        hv, wr, dyv = h_ref[...], w_ref[...], dy_ref[...]
        h1, h2, h3 = _shift_down(hv, 1), _shift_down(hv, 2), _shift_down(hv, 3)
        c = wr[3:4, :] * hv + wr[2:3, :] * h1 + wr[1:2, :] * h2 + wr[0:1, :] * h3
        s, dsilu = _silu_and_grad(c)
        if normalize:
            r = lax.rsqrt(jnp.sum(s * s, axis=-1, keepdims=True) + 1e-6)
            y = s * r
            dyv = dyv * scale
            ds = r * (dyv - y * jnp.sum(dyv * y, axis=-1, keepdims=True))
        else:
            ds = dyv
        dc = ds * dsilu
        dh = (wr[3:4, :] * dc + wr[2:3, :] * _shift_up(dc, 1) + wr[1:2, :] * _shift_up(dc, 2)
              + wr[0:1, :] * _shift_up(dc, 3))
        dh_ref[...] = dh.astype(BF16)
        dw_ref[0:1, :] = jnp.sum(dc * h3, axis=0, keepdims=True)
        dw_ref[1:2, :] = jnp.sum(dc * h2, axis=0, keepdims=True)
        dw_ref[2:3, :] = jnp.sum(dc * h1, axis=0, keepdims=True)
        dw_ref[3:4, :] = jnp.sum(dc * hv, axis=0, keepdims=True)

    return pl.pallas_call(
        body, name=name, grid=(ncols,),
        in_specs=[pl.BlockSpec((t, HEAD_A), lambda j: (0, j + col0)), pl.BlockSpec((CONV_A, HEAD_A), lambda j: (0, j + col0)),
                  pl.BlockSpec((t, HEAD_A), lambda j: (0, j))],
        out_specs=[pl.BlockSpec((t, HEAD_A), lambda j: (0, j)), pl.BlockSpec((CONV_A, HEAD_A), lambda j: (0, j))],
        out_shape=[jax.ShapeDtypeStruct((t, ncols * HEAD_A), BF16), jax.ShapeDtypeStruct((CONV_A, ncols * HEAD_A), F32)],
        compiler_params=_params(("parallel",)))(h, w, dy)


def _softplus(x):
    return jnp.maximum(x, 0.0) + jnp.log1p(jnp.exp(-jnp.abs(x)))


def _gates_fwd(ba, arow, brow, name):
    t = ba.shape[0]

    def body(x_ref, a_ref, b_ref, o_ref):
        xv = x_ref[...]
        lane = lax.broadcasted_iota(jnp.int32, xv.shape, 1)
        beta = _sigmoid(xv)
        g = -jnp.exp(a_ref[...]) * _softplus(xv + b_ref[...])
        o_ref[...] = jnp.where(lane < N_HEADS_A, beta, jnp.where(lane < 2 * N_HEADS_A, g, 0.0))

    return pl.pallas_call(body, name=name, out_shape=jax.ShapeDtypeStruct((t, LANE), F32),
                          compiler_params=_params())(ba, arow, brow)


def _gates_bwd(ba, arow, brow, dgb, name):
    t = ba.shape[0]

    def body(x_ref, a_ref, b_ref, d_ref, dx_ref, da_ref, db_ref):
        xv = x_ref[...]
        dv = d_ref[0] + d_ref[1] + d_ref[2] + d_ref[3]
        lane = lax.broadcasted_iota(jnp.int32, xv.shape, 1)
        beta = _sigmoid(xv)
        ea = jnp.exp(a_ref[...])
        z = xv + b_ref[...]
        dgv = jnp.where((lane >= N_HEADS_A) & (lane < 2 * N_HEADS_A), dv, 0.0) * (-ea)
        dz = dgv * _sigmoid(z)
        dx = jnp.where(lane < N_HEADS_A, dv * beta * (1.0 - beta), dz)
        dx_ref[...] = dx.astype(BF16)
        db_ref[...] = jnp.sum(dz, axis=0, keepdims=True)
        da_ref[...] = jnp.sum(dgv * _softplus(z), axis=0, keepdims=True)

    return pl.pallas_call(
        body, name=name,
        out_shape=[jax.ShapeDtypeStruct((t, LANE), BF16), jax.ShapeDtypeStruct((1, LANE), F32),
                   jax.ShapeDtypeStruct((1, LANE), F32)],
        compiler_params=_params())(ba, arow, brow, dgb)


def _dot(a, b, prec=None):
    if prec is None:
        return jnp.dot(a.astype(BF16), b.astype(BF16), preferred_element_type=F32)
    return jnp.dot(a, b, precision=prec, preferred_element_type=F32)


def _dot_nt(a, b, prec=None):
    if prec is None:
        a, b = a.astype(BF16), b.astype(BF16)
    return lax.dot_general(a, b, (((1,), (1,)), ((), ())), precision=prec, preferred_element_type=F32)


def _dot_tn(a, b, prec=None):
    if prec is None:
        a, b = a.astype(BF16), b.astype(BF16)
    return lax.dot_general(a, b, (((0,), (0,)), ((), ())), precision=prec, preferred_element_type=F32)


def _gdr_chunk_terms(k, beta, g):
    c = GDR_CHUNK
    row = lax.broadcasted_iota(jnp.int32, (c, c), 0)
    col = lax.broadcasted_iota(jnp.int32, (c, c), 1)
    causal, strict = row >= col, row > col
    gcum = _dot(causal.astype(F32), jnp.broadcast_to(g, (c, c)), HIGHEST)
    diff = gcum - gcum.T
    decay = jnp.where(causal, jnp.exp(jnp.where(causal, diff, 0.0)), 0.0)
    kb = k * beta
    kk = _dot_nt(kb, k)
    return row, col, causal, strict, gcum, decay, kb, kk


def _unit_lower_inverse(a):
    c = a.shape[0]
    eye = (lax.broadcasted_iota(jnp.int32, (c, c), 0) == lax.broadcasted_iota(jnp.int32, (c, c), 1)).astype(F32)
    p = -a
    inv = eye + p
    step = 1
    while 2 * step < c:
        p = _dot(p, p, HIGH)
        inv = inv + _dot(inv, p, HIGH)
        step *= 2
    return inv


def _head_gates(gates, head):
    lane = lax.broadcasted_iota(jnp.int32, gates.shape, 1)
    beta = jnp.sum(jnp.where(lane == head, gates, 0.0), axis=1, keepdims=True)
    g = jnp.sum(jnp.where(lane == head + N_HEADS_A, gates, 0.0), axis=1, keepdims=True)
    return beta, g


def _gdr_fwd(q, k, v, gates, name, comm=None):
    t = q.shape[0]
    c = GDR_CHUNK
    n = t // c

    hps = GDR_HEADS_PER_STEP

    def one_head(hh, q_ref, k_ref, v_ref, gb_ref, o_ref, tm_ref, s_ref, state):
        cols = slice(hh * HEAD_A, (hh + 1) * HEAD_A)
        qv, kv, vv = q_ref[:, cols], k_ref[:, cols], v_ref[:, cols]
        beta, g = _head_gates(gb_ref[...], pl.program_id(0) * hps + hh)
        row, col, causal, strict, gcum, decay, kb, kk = _gdr_chunk_terms(kv, beta, g)
        tm = _unit_lower_inverse(jnp.where(strict, kk * decay, 0.0))
        e = jnp.exp(gcum)
        u = _dot(tm, vv * beta, HIGH)
        w = _dot(tm, kb * e, HIGH)
        p = jnp.where(causal, _dot_nt(qv, kv) * decay, 0.0)
        s = state[hh]
        s_ref[hh, 0] = s
        tm_ref[hh, 0] = tm
        vn = u - _dot(w, s)
        o_ref[:, cols] = _dot(qv * e, s) + _dot(p, vn)
        glast = gcum[c - 1:c, :]
        state[hh] = s * jnp.exp(glast) + _dot_tn(kv * jnp.exp(glast - gcum), vn)

    def body(*refs):
        state = refs[-1]

        @pl.when(pl.program_id(1) == 0)
        def _():
            state[...] = jnp.zeros_like(state)

        for hh in range(hps):
            one_head(hh, *refs)

    blk = pl.BlockSpec((c, hps * HEAD_A), lambda h, i: (i, h))
    mat = pl.BlockSpec((hps, 1, c, c), lambda h, i: (h, i, 0, 0))
    return _call(
        body, name=name, grid=(N_HEADS_A // hps, n),
        in_specs=[blk, blk, blk, pl.BlockSpec((c, LANE), lambda h, i: (i, 0))],
        out_specs=[blk, mat, mat],
        out_shape=[jax.ShapeDtypeStruct((t, WIDTH_A), F32), jax.ShapeDtypeStruct((N_HEADS_A, n, c, c), F32),
                   jax.ShapeDtypeStruct((N_HEADS_A, n, HEAD_A, HEAD_A), F32)],
        scratch_shapes=[pltpu.VMEM((hps, HEAD_A, HEAD_A), F32)], sem=("parallel", "arbitrary"),
        args=(q, k, v, gates), comm=comm)


def _gdr_bwd(q, k, v, gates, tm_all, s_all, do, name, comm=None):
    t = q.shape[0]
    c = GDR_CHUNK
    n = t // c

    hps = GDR_HEADS_PER_STEP

    def one_head(hh, q_ref, k_ref, v_ref, gb_ref, tm_ref, s_ref, do_ref, dq_ref, dk_ref, dv_ref, dgb_ref, dstate):
        cols = slice(hh * HEAD_A, (hh + 1) * HEAD_A)
        qv, kv, vv, dov = q_ref[:, cols], k_ref[:, cols], v_ref[:, cols], do_ref[:, cols]
        head = pl.program_id(0) * hps + hh
        beta, g = _head_gates(gb_ref[...], head)
        tm, s, dsp = tm_ref[hh, 0], s_ref[hh, 0], dstate[hh]
        row, col, causal, strict, gcum, decay, kb, kk = _gdr_chunk_terms(kv, beta, g)
        e = jnp.exp(gcum)
        vb, kbe = vv * beta, kb * e
        u = _dot(tm, vb, HIGH)
        w = _dot(tm, kbe, HIGH)
        qk = _dot_nt(qv, kv)
        p = jnp.where(causal, qk * decay, 0.0)
        vn = u - _dot(w, s)
        glast = gcum[c - 1:c, :]
        el = jnp.exp(glast)
        f = jnp.exp(glast - gcum)
        kd = kv * f
        qe = qv * e

        dvn = _dot_tn(p, dov) + _dot(kd, dsp)
        dglast = el[:, 0:1] * jnp.sum(s * dsp, keepdims=True)
        dkd = _dot_nt(vn, dsp)
        dk = dkd * f
        df = jnp.sum(dkd * kv, axis=1, keepdims=True) * f[:, 0:1]
        dglast = dglast + jnp.sum(df, keepdims=True)
        dgc = -df
        dp = jnp.where(causal, _dot_nt(dov, vn), 0.0)
        dqe = _dot_nt(dov, s)
        dq = dqe * e
        de = jnp.sum(dqe * qv, axis=1, keepdims=True)
        dstate[hh] = dsp * el + _dot_tn(qe, dov) - _dot_tn(w, dvn)
        dw = -_dot_nt(dvn, s)
        dvb = _dot_tn(tm, dvn, HIGH)
        dkbe = _dot_tn(tm, dw, HIGH)
        da = -jnp.where(strict, _dot_nt(dvb, u) + _dot_nt(dkbe, w), 0.0)
        dkk = da * decay
        dqk = dp * decay
        dd = da * kk + dp * qk
        dq = dq + _dot(dqk, kv)
        dk = dk + _dot_tn(dqk, qv)
        dkb = _dot(dkk, kv) + dkbe * e
        dk = dk + _dot_tn(dkk, kb)
        de = de + jnp.sum(dkbe * kb, axis=1, keepdims=True)
        dk = dk + dkb * beta
        dbeta = jnp.sum(dkb * kv, axis=1, keepdims=True) + jnp.sum(dvb * vv, axis=1, keepdims=True)
        m = dd * decay
        dgc = dgc + jnp.sum(m, axis=1, keepdims=True) - jnp.sum(m.T, axis=1, keepdims=True)
        dgc = dgc + de * e[:, 0:1]
        dgc = dgc + jnp.where(row[:, 0:1] == c - 1, dglast, 0.0)
        dg = _dot((row <= col).astype(F32), jnp.broadcast_to(dgc, (c, c)), HIGHEST)
        dq_ref[:, cols] = dq
        dk_ref[:, cols] = dk
        dv_ref[:, cols] = dvb * beta
        lane = lax.broadcasted_iota(jnp.int32, (c, LANE), 1)
        dgb_ref[hh] = jnp.where(lane == head, dbeta, jnp.where(lane == head + N_HEADS_A, dg, 0.0))

    def body(*refs):
        dstate = refs[-1]

        @pl.when(pl.program_id(1) == 0)
        def _():
            dstate[...] = jnp.zeros_like(dstate)

        for hh in range(hps):
            one_head(hh, *refs)

    blk = pl.BlockSpec((c, hps * HEAD_A), lambda h, i: (n - 1 - i, h))
    mat = pl.BlockSpec((hps, 1, c, c), lambda h, i: (h, n - 1 - i, 0, 0))
    return _call(
        body, name=name, grid=(N_HEADS_A // hps, n),
        in_specs=[blk, blk, blk, pl.BlockSpec((c, LANE), lambda h, i: (n - 1 - i, 0)), mat, mat, blk],
        out_specs=[blk, blk, blk, pl.BlockSpec((hps, c, LANE), lambda h, i: (h, n - 1 - i, 0))],
        out_shape=[jax.ShapeDtypeStruct((t, WIDTH_A), F32)] * 3 + [jax.ShapeDtypeStruct((N_HEADS_A, t, LANE), F32)],
        scratch_shapes=[pltpu.VMEM((hps, HEAD_A, HEAD_A), F32)], sem=("parallel", "arbitrary"),
        args=(q, k, v, gates, tm_all, s_all, do), comm=comm)


def _onorm_fwd(o, gate, g, name):
    t = o.shape[0]

    def body(o_ref, gate_ref, g_ref, y_ref):
        ov, gv = o_ref[...], gate_ref[...]
        r = lax.rsqrt(jnp.mean(ov * ov, axis=-1, keepdims=True) + RMS_EPS)
        y_ref[...] = (ov * r * g_ref[...] * gv * _sigmoid(gv)).astype(BF16)

    blk = pl.BlockSpec((t, HEAD_A), lambda j: (0, j))
    return pl.pallas_call(
        body, name=name, grid=(N_HEADS_A,), in_specs=[blk, blk, pl.BlockSpec((1, HEAD_A), lambda j: (0, 0))],
        out_specs=blk, out_shape=jax.ShapeDtypeStruct((t, WIDTH_A), BF16),
        compiler_params=_params(("parallel",)))(o, gate, g)


def _onorm_bwd(o, gate, g, dy, name):
    t = o.shape[0]

    def body(o_ref, gate_ref, g_ref, dy_ref, do_ref, dgate_ref, dg_ref):
        @pl.when(pl.program_id(0) == 0)
        def _():
            dg_ref[...] = jnp.zeros_like(dg_ref)

        ov, gv, dyv = o_ref[...], gate_ref[...], dy_ref[...].astype(F32)
        r = lax.rsqrt(jnp.mean(ov * ov, axis=-1, keepdims=True) + RMS_EPS)
        oh = ov * r
        sg, dsg = _silu_and_grad(gv)
        dgate_ref[...] = (dyv * oh * g_ref[...] * dsg).astype(BF16)
        dn = dyv * sg
        dg_ref[...] += jnp.sum(dn * oh, axis=0, keepdims=True)
        dng = dn * g_ref[...]
        do_ref[...] = r * (dng - oh * jnp.mean(dng * oh, axis=-1, keepdims=True))

    blk = pl.BlockSpec((t, HEAD_A), lambda j: (0, j))
    vec = pl.BlockSpec((1, HEAD_A), lambda j: (0, 0))
    return pl.pallas_call(
        body, name=name, grid=(N_HEADS_A,), in_specs=[blk, blk, vec, blk], out_specs=[blk, blk, vec],
        out_shape=[jax.ShapeDtypeStruct((t, WIDTH_A), F32), jax.ShapeDtypeStruct((t, WIDTH_A), BF16),
                   jax.ShapeDtypeStruct((1, HEAD_A), F32)],
        compiler_params=_params(("arbitrary",)))(o, gate, g, dy)


def _cmul(ar, ai, br, bi):
    return ar * br - ai * bi, ar * bi + ai * br


def _scan_tables(ar, ai, reverse):
    p1 = (ar, ai)
    p2 = _cmul(*p1, *p1)
    p4 = _cmul(*p2, *p2)
    p8 = _cmul(*p4, *p4)
    p3 = _cmul(*p2, *p1)
    p5 = _cmul(*p4, *p1)
    p6 = _cmul(*p4, *p2)
    p7 = _cmul(*p4, *p3)
    pows = [p1, p2, p3, p4, p5, p6, p7, p8]
    rows = lax.broadcasted_iota(jnp.int32, (8, ar.shape[1]), 0)
    tr = jnp.zeros((8, ar.shape[1]), F32)
    ti = jnp.zeros((8, ar.shape[1]), F32)
    for r in range(8):
        pw = pows[7 - r] if reverse else pows[r]
        tr = jnp.where(rows == r, pw[0], tr)
        ti = jnp.where(rows == r, pw[1], ti)
    return p1, p2, p4, p8, tr, ti


def _tile_scan(xr, xi, p1, p2, p4, reverse):
    rows = lax.broadcasted_iota(jnp.int32, xr.shape, 0)
    for s, (pr, pi) in ((1, p1), (2, p2), (4, p4)):
        if reverse:
            keep = rows < 8 - s
            sr, si = pltpu.roll(xr, 8 - s, 0), pltpu.roll(xi, 8 - s, 0)
        else:
            keep = rows >= s
            sr, si = pltpu.roll(xr, s, 0), pltpu.roll(xi, s, 0)
        sr, si = jnp.where(keep, sr, 0.0), jnp.where(keep, si, 0.0)
        mr, mi = _cmul(pr, pi, sr, si)
        xr, xi = xr + mr, xi + mi
    return xr, xi


def _s5_scan_fwd(bu, a, name, tb=512, comm=None):
    t = bu.shape[0]
    cb = SCAN_CB
    nt = t // tb

    def body(b_ref, a_ref, x_ref, carry):
        @pl.when(pl.program_id(1) == 0)
        def _():
            carry[...] = jnp.zeros_like(carry)

        ar, ai = a_ref[:, 0:cb], a_ref[:, cb:2 * cb]
        p1, p2, p4, p8, tr, ti = _scan_tables(ar, ai, False)

        def step(j, c):
            cr, ci = c
            i = pl.multiple_of(j * 8, 8)
            xr, xi = _tile_scan(b_ref[pl.ds(i, 8), 0:cb], b_ref[pl.ds(i, 8), cb:2 * cb], p1, p2, p4, False)
            mr, mi = _cmul(tr, ti, cr, ci)
            xr, xi = xr + mr, xi + mi
            x_ref[pl.ds(i, 8), 0:cb] = xr
            x_ref[pl.ds(i, 8), cb:2 * cb] = xi
            return xr[7:8, :], xi[7:8, :]

        cr, ci = lax.fori_loop(0, tb // 8, step, (carry[0:1, :], carry[1:2, :]), unroll=2)
        carry[0:1, :] = cr
        carry[1:2, :] = ci

    blk = pl.BlockSpec((tb, 2 * cb), lambda j, i: (i, j))
    return _call(
        body, name=name, grid=(SSM_CH // cb, nt),
        in_specs=[blk, pl.BlockSpec((1, 2 * cb), lambda j, i: (0, j))], out_specs=blk,
        out_shape=jax.ShapeDtypeStruct((t, 2 * SSM_CH), F32), scratch_shapes=[pltpu.VMEM((8, cb), F32)],
        sem=("parallel", "arbitrary"), args=(bu, a), comm=comm)


def _s5_scan_bwd(dx, x, a, name, tb=512, comm=None):
    t = dx.shape[0]
    cb = SCAN_CB
    nt = t // tb
    nj = tb // 8

    def body(d_ref, x_ref, xp_ref, a_ref, l_ref, da_ref, carry, acc):
        tblk = pl.program_id(1)

        @pl.when(tblk == 0)
        def _():
            carry[...] = jnp.zeros_like(carry)
            acc[...] = jnp.zeros_like(acc)

        ar, ai = a_ref[:, 0:cb], a_ref[:, cb:2 * cb]
        p1, p2, p4, p8, tr, ti = _scan_tables(ar, -ai, True)
        rows = lax.broadcasted_iota(jnp.int32, (8, cb), 0)

        def step(jj, c):
            cr, ci, sr_acc, si_acc = c
            j = nj - 1 - jj
            i = pl.multiple_of(j * 8, 8)
            lr, li = _tile_scan(d_ref[pl.ds(i, 8), 0:cb], d_ref[pl.ds(i, 8), cb:2 * cb], p1, p2, p4, True)
            mr, mi = _cmul(tr, ti, cr, ci)
            lr, li = lr + mr, li + mi
            l_ref[pl.ds(i, 8), 0:cb] = lr
            l_ref[pl.ds(i, 8), cb:2 * cb] = li
            ip = pl.multiple_of(jnp.maximum(j - 1, 0) * 8, 8)
            prev_r = jnp.where(j > 0, x_ref[pl.ds(ip, 8), 0:cb], xp_ref[:, 0:cb])
            prev_i = jnp.where(j > 0, x_ref[pl.ds(ip, 8), cb:2 * cb], xp_ref[:, cb:2 * cb])
            edge = jnp.where(jnp.logical_and(j == 0, tblk == nt - 1), 0.0, 1.0)
            xs_r = jnp.where(rows == 0, pltpu.roll(prev_r, 1, 0) * edge, pltpu.roll(x_ref[pl.ds(i, 8), 0:cb], 1, 0))
            xs_i = jnp.where(rows == 0, pltpu.roll(prev_i, 1, 0) * edge, pltpu.roll(x_ref[pl.ds(i, 8), cb:2 * cb], 1, 0))
            sr_acc = sr_acc + lr * xs_r + li * xs_i
            si_acc = si_acc + li * xs_r - lr * xs_i
            return lr[0:1, :], li[0:1, :], sr_acc, si_acc

        cr, ci, sr_acc, si_acc = lax.fori_loop(
            0, nj, step, (carry[0:1, :], carry[1:2, :], acc[:, 0:cb], acc[:, cb:2 * cb]))
        carry[0:1, :] = cr
        carry[1:2, :] = ci
        acc[:, 0:cb] = sr_acc
        acc[:, cb:2 * cb] = si_acc

        @pl.when(tblk == nt - 1)
        def _():
            da_ref[...] = jnp.sum(acc[...], axis=0, keepdims=True)

    blk = pl.BlockSpec((tb, 2 * cb), lambda j, i: (nt - 1 - i, j))
    prev = pl.BlockSpec((8, 2 * cb), lambda j, i: (jnp.maximum((nt - 1 - i) * (tb // 8) - 1, 0), j))
    vec = pl.BlockSpec((1, 2 * cb), lambda j, i: (0, j))
    return _call(
        body, name=name, grid=(SSM_CH // cb, nt), in_specs=[blk, blk, prev, vec], out_specs=[blk, vec],
        out_shape=[jax.ShapeDtypeStruct((t, 2 * SSM_CH), F32), jax.ShapeDtypeStruct((1, 2 * SSM_CH), F32)],
        scratch_shapes=[pltpu.VMEM((8, cb), F32), pltpu.VMEM((8, 2 * cb), F32)],
        sem=("parallel", "arbitrary"), args=(dx, x, x, a), comm=comm)


def _glu_fwd(yc, u, dvec, wg, bg, name, tr=256):
    t = yc.shape[0]

    def body(yc_ref, u_ref, d_ref, w_ref, b_ref, yl_ref, yb_ref):
        yl = yc_ref[...] + d_ref[...] * u_ref[...]
        yl_ref[...] = yl
        yg, _ = _gelu_and_grad(yl)
        z = jnp.dot(yg.astype(BF16), w_ref[...], preferred_element_type=F32) + b_ref[...]
        yb_ref[...] = (yg * _sigmoid(z)).astype(BF16)

    blk = pl.BlockSpec((tr, SSM_WIDTH), lambda i: (i, 0))
    vec = pl.BlockSpec((1, SSM_WIDTH), lambda i: (0, 0))
    return pl.pallas_call(
        body, name=name, grid=(t // tr,),
        in_specs=[blk, blk, vec, pl.BlockSpec((SSM_WIDTH, SSM_WIDTH), lambda i: (0, 0)), vec],
        out_specs=[blk, blk],
        out_shape=[jax.ShapeDtypeStruct((t, SSM_WIDTH), F32), jax.ShapeDtypeStruct((t, SSM_WIDTH), BF16)],
        compiler_params=_params(("parallel",)))(yc, u, dvec, wg, bg)


def _glu_bwd(yl, u, dvec, wg, bg, dyb, name, tr=256):
    t = yl.shape[0]

    def body(yl_ref, u_ref, d_ref, w_ref, b_ref, dy_ref, dyl_ref, du_ref, dw_ref, db_ref, dd_ref):
        @pl.when(pl.program_id(0) == 0)
        def _():
            dw_ref[...] = jnp.zeros_like(dw_ref)
            db_ref[...] = jnp.zeros_like(db_ref)
            dd_ref[...] = jnp.zeros_like(dd_ref)

        ylv, dyv, wv = yl_ref[...], dy_ref[...].astype(F32), w_ref[...]
        yg, dgelu = _gelu_and_grad(ylv)
        ygb = yg.astype(BF16)
        z = jnp.dot(ygb, wv, preferred_element_type=F32) + b_ref[...]
        sg = _sigmoid(z)
        dz = dyv * yg * sg * (1.0 - sg)
        dzb = dz.astype(BF16)
        dyg = dyv * sg + lax.dot_general(dzb, wv, (((1,), (1,)), ((), ())), preferred_element_type=F32)
        dyl = dyg * dgelu
        dyl_ref[...] = dyl.astype(BF16)
        du_ref[...] = dyl * d_ref[...]
        dw_ref[...] += lax.dot_general(ygb, dzb, (((0,), (0,)), ((), ())), preferred_element_type=F32)
        db_ref[...] += jnp.sum(dz, axis=0, keepdims=True)
        dd_ref[...] += jnp.sum(dyl * u_ref[...], axis=0, keepdims=True)

    blk = pl.BlockSpec((tr, SSM_WIDTH), lambda i: (i, 0))
    vec = pl.BlockSpec((1, SSM_WIDTH), lambda i: (0, 0))
    wsp = pl.BlockSpec((SSM_WIDTH, SSM_WIDTH), lambda i: (0, 0))
    return pl.pallas_call(
        body, name=name, grid=(t // tr,), in_specs=[blk, blk, vec, wsp, vec, blk],
        out_specs=[blk, blk, wsp, vec, vec],
        out_shape=[jax.ShapeDtypeStruct((t, SSM_WIDTH), BF16), jax.ShapeDtypeStruct((t, SSM_WIDTH), F32),
                   jax.ShapeDtypeStruct((SSM_WIDTH, SSM_WIDTH), F32), jax.ShapeDtypeStruct((1, SSM_WIDTH), F32),
                   jax.ShapeDtypeStruct((1, SSM_WIDTH), F32)],
        compiler_params=_params(("arbitrary",)))(yl, u, dvec, wg, bg, dyb)


def _mesh_pos():
    return lax.axis_index("x"), lax.axis_index("y"), lax.axis_index("c")


def _device_index():
    x, y, c = _mesh_pos()
    return 4 * x + 2 * y + c


def _gather_comm(arrays):
    na = len(arrays)

    def own_copy(ins, outs, sems, ai):
        return pltpu.make_async_copy(ins[ai], outs[ai].at[_device_index()], sems[2].at[ai])

    def ctx(ins, outs, sems):
        send_sems, recv_sems = sems[:2]
        x, y, c = _mesh_pos()
        chips = [(1 - x, y), (x, 1 - y), (1 - x, 1 - y)]

        def copy(ai, kk, block, to, own=False):
            slot = outs[ai].at[4 * block[0] + 2 * block[1] + block[2]]
            return pltpu.make_async_remote_copy(
                src_ref=ins[ai] if own else slot, dst_ref=slot, send_sem=send_sems.at[ai, kk],
                recv_sem=recv_sems.at[ai, kk], device_id=to, device_id_type=MESH)

        return (x, y, c), (x, y, 1 - c), chips, c, copy

    def start(ins, outs, sems):
        me, sibling, chips, c, copy = ctx(ins, outs, sems)
        for ai in range(na):
            copy(ai, 0, me, sibling, own=True).start()
            for j, chip in enumerate(chips):
                copy(ai, 1 + j, me, (*chip, c), own=True).start()
        for ai in range(na):
            own_copy(ins, outs, sems, ai).start()

    def mid(ins, outs, sems):
        me, sibling, chips, c, copy = ctx(ins, outs, sems)
        for ai in range(na):
            for j, chip in enumerate(chips):
                copy(ai, 1 + j, (*chip, c), me).wait_recv()
                copy(ai, 4 + j, (*chip, c), sibling).start()

    def end(ins, outs, sems):
        me, sibling, chips, c, copy = ctx(ins, outs, sems)
        for ai in range(na):
            copy(ai, 0, sibling, me).wait_recv()
            copy(ai, 0, me, sibling, own=True).wait_send()
            for j, chip in enumerate(chips):
                copy(ai, 4 + j, (*chip, 1 - c), me).wait_recv()
                copy(ai, 1 + j, me, (*chip, c), own=True).wait_send()
                copy(ai, 4 + j, (*chip, c), sibling).wait_send()
            own_copy(ins, outs, sems, ai).wait()

    return Comm(arrays, [jax.ShapeDtypeStruct((N_DEV,) + a.shape, a.dtype) for a in arrays],
                [pltpu.SemaphoreType.DMA((na, 7)), pltpu.SemaphoreType.DMA((na, 7)), pltpu.SemaphoreType.DMA((na,))],
                start, end, mid)


def _swap_comm(arrays):
    na = len(arrays)
    offs = np.concatenate([[0], np.cumsum([a.shape[1] for a in arrays])]).astype(int)

    def copies(ins, outs, sems):
        x, y, c = _mesh_pos()
        return [pltpu.make_async_remote_copy(
            src_ref=ins[ai].at[2 * k + 1 - c], dst_ref=outs[0].at[k, pl.ds(int(offs[ai]), arrays[ai].shape[1])],
            send_sem=sems[0].at[ai, k], recv_sem=sems[1].at[ai, k], device_id=(x, y, 1 - c), device_id_type=MESH)
            for ai in range(na) for k in range(4)]

    def start(ins, outs, sems):
        for cp in copies(ins, outs, sems):
            cp.start()

    def end(ins, outs, sems):
        for cp in copies(ins, outs, sems):
            cp.wait()

    return Comm(arrays, [jax.ShapeDtypeStruct((4, int(offs[-1]), PACK_COLS), arrays[0].dtype)],
                [pltpu.SemaphoreType.DMA((na, 4)), pltpu.SemaphoreType.DMA((na, 4))], start, end)


def _chips_comm(send):
    def copies(ins, outs, sems):
        x, y, c = _mesh_pos()
        chips = [(1 - x, y), (x, 1 - y), (1 - x, 1 - y)]
        return [pltpu.make_async_remote_copy(
            src_ref=ins[0].at[2 * cx + cy], dst_ref=outs[0].at[j], send_sem=sems[0].at[j], recv_sem=sems[1].at[j],
            device_id=(cx, cy, c), device_id_type=MESH) for j, (cx, cy) in enumerate(chips)]

    def start(ins, outs, sems):
        for cp in copies(ins, outs, sems):
            cp.start()

    def end(ins, outs, sems):
        for cp in copies(ins, outs, sems):
            cp.wait()

    return Comm([send], [jax.ShapeDtypeStruct((3,) + send.shape[1:], send.dtype)],
                [pltpu.SemaphoreType.DMA((3,)), pltpu.SemaphoreType.DMA((3,))], start, end)


def _all_gather(arrays, name):
    na = len(arrays)

    def body(*refs):
        ins, outs = refs[:na], refs[na:2 * na]
        send_sems, recv_sems, local_sems = refs[2 * na:]
        x, y, c = _mesh_pos()
        me, sibling = (x, y, c), (x, y, 1 - c)
        chips = [(1 - x, y), (x, 1 - y), (1 - x, 1 - y)]
        waits = []
        for ai in range(na):
            in_ref, out_ref = ins[ai], outs[ai]

            def slot(px, py, pc, out_ref=out_ref):
                return out_ref.at[4 * px + 2 * py + pc]

            def copy(kk, block, to, src=None, ai=ai, slot=slot):
                return pltpu.make_async_remote_copy(
                    src_ref=slot(*block) if src is None else src, dst_ref=slot(*block),
                    send_sem=send_sems.at[ai, kk], recv_sem=recv_sems.at[ai, kk], device_id=to, device_id_type=MESH)

            mine = pltpu.make_async_copy(in_ref, slot(*me), local_sems.at[ai])
            mine.start()
            first = [copy(0, me, sibling, src=in_ref)]
            first += [copy(1 + j, me, (*chip, c), src=in_ref) for j, chip in enumerate(chips)]
            for cp in first:
                cp.start()
            waits.append((copy, mine, first))
        sends = []
        for ai in range(na):
            copy, mine, first = waits[ai]
            passed = [copy(4 + j, (*chip, c), sibling) for j, chip in enumerate(chips)]
            for j, chip in enumerate(chips):
                copy(1 + j, (*chip, c), me).wait_recv()
                passed[j].start()
            sends.append(passed)
        for ai in range(na):
            copy, mine, first = waits[ai]
            copy(0, sibling, me).wait_recv()
            for j, chip in enumerate(chips):
                copy(4 + j, (*chip, 1 - c), me).wait_recv()
            for cp in first + sends[ai]:
                cp.wait_send()
            mine.wait()

    any_spec = pl.BlockSpec(memory_space=pl.ANY)
    return pl.pallas_call(
        body, name=name, in_specs=[any_spec] * na, out_specs=[any_spec] * na,
        out_shape=[jax.ShapeDtypeStruct((N_DEV,) + a.shape, a.dtype) for a in arrays],
        scratch_shapes=[pltpu.SemaphoreType.DMA((na, 7)), pltpu.SemaphoreType.DMA((na, 7)),
                        pltpu.SemaphoreType.DMA((na,))],
        compiler_params=pltpu.CompilerParams(has_side_effects=True))(*arrays)


def _swap_sibling(arrays, name):
    na = len(arrays)
    offs = np.concatenate([[0], np.cumsum([a.shape[1] for a in arrays])]).astype(int)
    rows = int(offs[-1])

    def body(*refs):
        ins, recv_ref = refs[:na], refs[na]
        send_sems, recv_sems = refs[na + 1:]
        x, y, c = _mesh_pos()
        started = []
        for ai in range(na):
            span = pl.ds(int(offs[ai]), arrays[ai].shape[1])
            for k in range(4):
                remote = pltpu.make_async_remote_copy(
                    src_ref=ins[ai].at[2 * k + 1 - c], dst_ref=recv_ref.at[k, span], send_sem=send_sems.at[ai, k],
                    recv_sem=recv_sems.at[ai, k], device_id=(x, y, 1 - c), device_id_type=MESH)
                remote.start()
                started.append(remote)
        for remote in started:
            remote.wait()

    any_spec = pl.BlockSpec(memory_space=pl.ANY)
    return pl.pallas_call(
        body, name=name, in_specs=[any_spec] * na, out_specs=any_spec,
        out_shape=jax.ShapeDtypeStruct((4, rows, PACK_COLS), arrays[0].dtype),
        scratch_shapes=[pltpu.SemaphoreType.DMA((na, 4)), pltpu.SemaphoreType.DMA((na, 4))])(*arrays)


def _exchange_chips(send, name):
    def body(s_ref, o_ref, send_sems, recv_sems):
        x, y, c = _mesh_pos()
        chips = [(1 - x, y), (x, 1 - y), (1 - x, 1 - y)]
        cps = [pltpu.make_async_remote_copy(
            src_ref=s_ref.at[2 * cx + cy], dst_ref=o_ref.at[j], send_sem=send_sems.at[j], recv_sem=recv_sems.at[j],
            device_id=(cx, cy, c), device_id_type=MESH) for j, (cx, cy) in enumerate(chips)]
        for cp in cps:
            cp.start()
        for cp in cps:
            cp.wait()

    any_spec = pl.BlockSpec(memory_space=pl.ANY)
    return pl.pallas_call(
        body, name=name, in_specs=[any_spec], out_specs=any_spec,
        out_shape=jax.ShapeDtypeStruct((3,) + send.shape[1:], send.dtype),
        scratch_shapes=[pltpu.SemaphoreType.DMA((3,)), pltpu.SemaphoreType.DMA((3,))])(send)


def _pair_sum(keep, recv, name, tr=464):
    nchip, rows, cols = keep.shape

    def body(g_ref, r_ref, o_ref):
        o_ref[...] = (g_ref[...].astype(F32) + r_ref[...].astype(F32)).astype(BF16)

    blk = pl.BlockSpec((1, tr, cols), lambda k, i: (k, i, 0))
    return pl.pallas_call(
        body, name=name, grid=(nchip, rows // tr), in_specs=[blk, blk], out_specs=blk,
        out_shape=jax.ShapeDtypeStruct((nchip, rows, cols), BF16),
        compiler_params=_params(("parallel", "parallel")))(keep, recv)


def _chip_sum(own, others, name, tr=464):
    _, rows, cols = own.shape
    chip = (2 * lax.axis_index("x") + lax.axis_index("y")).astype(jnp.int32).reshape(1)

    def body(chip_ref, own_ref, oth_ref, o_ref):
        del chip_ref
        acc = own_ref[0].astype(F32)
        for j in range(3):
            acc = acc + oth_ref[j].astype(F32)
        o_ref[...] = acc

    grid_spec = pltpu.PrefetchScalarGridSpec(
        num_scalar_prefetch=1, grid=(rows // tr,),
        in_specs=[pl.BlockSpec((1, tr, cols), lambda i, chip_ref: (chip_ref[0], i, 0)),
                  pl.BlockSpec((3, tr, cols), lambda i, chip_ref: (0, i, 0))],
        out_specs=pl.BlockSpec((tr, cols), lambda i, chip_ref: (i, 0)))
    return pl.pallas_call(
        body, name=name, grid_spec=grid_spec, out_shape=jax.ShapeDtypeStruct((rows, cols), F32),
        compiler_params=_params(("parallel",)))(chip, own, others)


def _sum_leading(parts, name, tr=464):
    nparts, rows, cols = parts.shape
    tr = tr if rows % tr == 0 else rows

    def body(p_ref, o_ref):
        acc = p_ref[0].astype(F32)
        for i in range(1, nparts):
            acc = acc + p_ref[i].astype(F32)
        o_ref[...] = acc

    return pl.pallas_call(
        body, name=name, grid=(rows // tr,),
        in_specs=[pl.BlockSpec((nparts, tr, cols), lambda i: (0, i, 0))],
        out_specs=pl.BlockSpec((tr, cols), lambda i: (i, 0)), out_shape=jax.ShapeDtypeStruct((rows, cols), F32),
        compiler_params=_params(("parallel",)))(parts)


def _adamw(w, g, m, v, name):
    shape = w.shape
    cols = shape[-1]
    rows = int(np.prod(shape[:-1])) if len(shape) > 1 else 1
    w2, g2, m2, v2 = (a.reshape(rows, cols) for a in (w, g, m, v))
    tr = rows
    for cand in (512, 256, 128, 64, 32, 16, 8):
        if rows % cand == 0 and rows > cand:
            tr = cand
            break
    bc1, bc2 = 1.0 - ADAM_B1 ** ADAM_STEP, 1.0 - ADAM_B2 ** ADAM_STEP

    def body(w_ref, g_ref, m_ref, v_ref, d_ref, nm_ref, nv_ref):
        gv = g_ref[...]
        nm = ADAM_B1 * m_ref[...] + (1.0 - ADAM_B1) * gv
        nv = ADAM_B2 * v_ref[...] + (1.0 - ADAM_B2) * (gv * gv)
        nm_ref[...] = nm
        nv_ref[...] = nv
        d_ref[...] = -ADAM_LR * ((nm / bc1) / (jnp.sqrt(nv / bc2) + ADAM_EPS) + ADAM_WD * w_ref[...])

    blk = pl.BlockSpec((tr, cols), lambda i: (i, 0))
    outs = pl.pallas_call(
        body, name=name, grid=(rows // tr,), in_specs=[blk] * 4, out_specs=[blk] * 3,
        out_shape=[jax.ShapeDtypeStruct((rows, cols), F32)] * 3, compiler_params=_params(("parallel",)))(w2, g2, m2, v2)
    return tuple(o.reshape(shape) for o in outs)


WEIGHT_NAMES = ['norm_mix_g', 'norm_xa_g', 'norm_ffn_g', 'norm_mem_g', 'norm_final_g', 'w_in_ab', 'conv_qkv_a',
                'a_log_a', 'dt_bias_a', 'onorm_g_a', 'ssm_lambda_re', 'ssm_lambda_im', 'ssm_b_re', 'ssm_b_im',
                'ssm_c_re', 'ssm_c_im', 'ssm_d', 'ssm_log_dt', 'w_glu_b', 'b_glu_b', 'w_out_ab', 'pool_w',
                'pool_scale', 'xa_wq', 'xa_wkv', 'xa_wo', 'ffn_w_up', 'ffn_conv', 'ffn_w_down']
BIG_SHARDED = {'w_in_ab': ((1, 1024, 2568), 2), 'w_glu_b': ((1, 512, 512), 1), 'w_out_ab': ((1, 1024, 1024), 1),
               'pool_w': ((1, 4, 256, 256), 2), 'xa_wq': ((2, 1024, 1024), 1), 'xa_wkv': ((2, 1024, 2048), 2),
               'xa_wo': ((2, 1024, 1024), 1), 'ffn_w_up': ((2, 1024, 5632), 2), 'ffn_w_down': ((2, 2816, 1024), 1)}
SMALL_SHARDED = {'conv_qkv_a': ((1, 4, 1536), 2), 'pool_scale': ((1, 1024), 1), 'ffn_conv': ((2, 3, 5632), 2)}
REPLICATED = {'norm_mix_g': (2, 1024), 'norm_xa_g': (2, 1024), 'norm_ffn_g': (2, 1024), 'norm_mem_g': (1024,),
              'norm_final_g': (1024,), 'a_log_a': (1, 4), 'dt_bias_a': (1, 4), 'onorm_g_a': (1, 128),
              'ssm_lambda_re': (1, 32, 64), 'ssm_lambda_im': (1, 32, 64), 'ssm_b_re': (1, 32, 64, 16),
              'ssm_b_im': (1, 32, 64, 16), 'ssm_c_re': (1, 32, 16, 64), 'ssm_c_im': (1, 32, 16, 64),
              'ssm_d': (1, 32, 16), 'ssm_log_dt': (1, 32), 'b_glu_b': (1, 512)}
PACK_ROW_ALIGN = 8


def _shard_shape(shape, axis):
    return tuple(s // N_DEV if i == axis else s for i, s in enumerate(shape))


def _round_up(n, m):
    return (n + m - 1) // m * m


def _pack(arrays):
    total = sum(int(np.prod(a.shape)) for a in arrays)
    padded = _round_up(total, PACK_COLS * PACK_ROW_ALIGN)
    parts = [a.astype(F32).reshape(-1) for a in arrays]
    if padded != total:
        parts.append(jnp.zeros((padded - total,), F32))
    return jnp.concatenate(parts).reshape(padded // PACK_COLS, PACK_COLS)


def _unpack(packed, shapes):
    flat, out, off = packed.reshape(-1), [], 0
    for shape in shapes:
        size = int(np.prod(shape))
        out.append(flat[off:off + size].reshape(shape))
        off += size
    return out


def _split_shards(full, axis):
    shape = full.shape
    s = shape[axis] // N_DEV
    a = full.reshape(shape[:axis] + (N_DEV, s) + shape[axis + 1:])
    return jnp.moveaxis(a, axis, 0).reshape(N_DEV, -1)


def _merge_shards(pieces, shape, axis):
    sh = _shard_shape(shape, axis)
    a = pieces.reshape((N_DEV,) + sh)
    a = jnp.moveaxis(a, 0, axis)
    return a.reshape(shape)


_SCAN_NB = SSM_CH // SCAN_CB


def _to_scan_layout(m, axis):
    shape = m.shape
    m = m.reshape(shape[:axis] + (2, _SCAN_NB, SCAN_CB) + shape[axis + 1:])
    return jnp.swapaxes(m, axis, axis + 1).reshape(shape)


def _from_scan_layout(m, axis):
    shape = m.shape
    m = m.reshape(shape[:axis] + (_SCAN_NB, 2, SCAN_CB) + shape[axis + 1:])
    return jnp.swapaxes(m, axis, axis + 1).reshape(shape)


def _s5_discretise(lam_re, lam_im, b_re, b_im, log_dt):
    dt = jnp.exp(log_dt)[:, None]
    mag = jnp.exp(lam_re * dt)
    ang = lam_im * dt
    lb_re, lb_im = mag * jnp.cos(ang), mag * jnp.sin(ang)
    den = lam_re * lam_re + lam_im * lam_im
    nr, ni = lb_re - 1.0, lb_im
    coef_re = (nr * lam_re + ni * lam_im) / den
    coef_im = (ni * lam_re - nr * lam_im) / den
    bb_re = coef_re[..., None] * b_re - coef_im[..., None] * b_im
    bb_im = coef_re[..., None] * b_im + coef_im[..., None] * b_re
    return lb_re, lb_im, bb_re, bb_im


_GROUPS_PER_BLOCK = N_GROUPS // _SCAN_NB
_U_BLOCK = _GROUPS_PER_BLOCK * SSM_GROUP


def _s5_matrices(lb_re, lb_im, bb_re, bb_im, c_re, c_im):
    eye = jnp.eye(_GROUPS_PER_BLOCK, dtype=F32)
    blocked = lambda m: m.reshape((_SCAN_NB, _GROUPS_PER_BLOCK) + m.shape[1:])
    bmat = lambda bb: jnp.einsum('jgph,gk->jghkp', blocked(bb), eye).reshape(_SCAN_NB, _U_BLOCK, SCAN_CB)
    cmat = lambda cc: jnp.einsum('jghp,gk->jkpgh', blocked(cc), eye).reshape(_SCAN_NB, SCAN_CB, _U_BLOCK)
    b_in = jnp.concatenate([bmat(bb_re), bmat(bb_im)], axis=2)
    c_out = jnp.concatenate([cmat(c_re), -cmat(c_im)], axis=1)
    a_row = _to_scan_layout(jnp.concatenate([lb_re.reshape(1, SSM_CH), lb_im.reshape(1, SSM_CH)], axis=1), 1)
    return b_in, c_out, a_row


def _s5_matrix_grads(db_in, dc_out, da_row):
    da_nat = _from_scan_layout(da_row, 1)
    eye = jnp.eye(_GROUPS_PER_BLOCK, dtype=F32)
    nb, gb = _SCAN_NB, _GROUPS_PER_BLOCK
    bgrad = lambda m: jnp.einsum('jghkp,gk->jgph', m.reshape(nb, gb, SSM_GROUP, gb, SSM_STATE), eye
                                 ).reshape(N_GROUPS, SSM_STATE, SSM_GROUP)
    cgrad = lambda m: jnp.einsum('jkpgh,gk->jghp', m.reshape(nb, gb, SSM_STATE, gb, SSM_GROUP), eye
                                 ).reshape(N_GROUPS, SSM_GROUP, SSM_STATE)
    dbb_re, dbb_im = bgrad(db_in[:, :, :SCAN_CB]), bgrad(db_in[:, :, SCAN_CB:])
    dc_re, dc_im = cgrad(dc_out[:, :SCAN_CB]), -cgrad(dc_out[:, SCAN_CB:])
    dlb_re = da_nat[0, :SSM_CH].reshape(N_GROUPS, SSM_STATE)
    dlb_im = da_nat[0, SSM_CH:].reshape(N_GROUPS, SSM_STATE)
    return dlb_re, dlb_im, dbb_re, dbb_im, dc_re, dc_im


def _as_pieces(a):
    return a.reshape(N_DEV, a.shape[0] // N_DEV, a.shape[1])


def _hybrid_fwd(xn, x, wts, p, weights, riders):
    sv = {}
    hq = _mm(xn, wts['w_qkv_t'], "nt", "l0_in_qkv")
    gate = _mm(xn, wts['w_gate_t'], "nt", "l0_in_gate")
    ba = _mm(xn, wts['w_ba_t'], "nt", "l0_in_ba")
    u = _mm(xn, wts['w_u_t'], "nt", "l0_in_u")
    conv = p['conv_qkv']
    q = _qkv_pre_fwd(hq, conv, 0, 4, True, HEAD_A ** -0.5, "l0_q_pre")
    k = _qkv_pre_fwd(hq, conv, 4, 4, True, 1.0, "l0_k_pre")
    v = _qkv_pre_fwd(hq, conv, 8, 4, False, 1.0, "l0_v_pre")
    gates = _gates_fwd(ba, p['arow'], p['brow'], "l0_gates")
    o, tm_all, s_all = riders.run("l0_gdr_fwd", _gdr_fwd, q, k, v, gates)
    wts['w_glu'], wts['w_out'] = weights.full['w_glu'], weights.full['w_out']
    y_a = _onorm_fwd(o, gate, p['onorm_g'], "l0_onorm")
    bu = riders.run("l0_s5_bu", _mm_bd, u, p['b_in'], "nn")
    xs = riders.run("l0_s5_scan", _s5_scan_fwd, bu, p['a_row'])
    yc = riders.run("l0_s5_cx", _mm_bd, xs, p['c_out'], "nn")
    yl, y_b = _glu_fwd(yc, u, p['d_row'], wts['w_glu'], p['b_glu'], "l0_glu")
    mixed = jnp.concatenate([y_a, y_b], axis=1)
    x1 = _mm(mixed, wts['w_out'], "nn", "l0_out", res=x)
    sv.update(hq=hq, gate=gate, ba=ba, u=u, q=q, k=k, v=v, gb=gates, o=o, tm=tm_all, s=s_all, xs=xs, yl=yl, mixed=mixed)
    return x1, sv


def _hybrid_bwd(dx1, xn, wts, p, sv, riders):
    gr = {}
    dmixed = _mm(dx1, wts['w_out'], "nt", "l0_out_dx", out_dtype=BF16)
    riders.grad('w_out', _as_pieces(_mm(sv['mixed'], dx1, "tn", "l0_out_dw", out_dtype=BF16)))
    dya, dyb = dmixed[:, :WIDTH_A], dmixed[:, WIDTH_A:]
    dyl, du_direct, dw_glu, gr['b_glu_b'], dd = _glu_bwd(
        sv['yl'], sv['u'], p['d_row'], wts['w_glu'], p['b_glu'], dyb, "l0_glu_bwd")
    riders.grad('w_glu', dw_glu.astype(BF16).reshape(N_DEV, -1, PACK_COLS))
    dxs = riders.run("l0_s5_cx_dx", _mm_bd, dyl, p['c_out'], "nt")
    dc_out = _mm_bd(sv['xs'], dyl, "tn", "l0_s5_cx_dw")
    lam, da_row = riders.run("l0_s5_scan_bwd", _s5_scan_bwd, dxs, sv['xs'], p['a_row'])
    du = _mm_bd(lam, p['b_in'], "nt", "l0_s5_bu_dx", res=du_direct, out_dtype=BF16)
    db_in = _mm_bd(sv['u'], lam, "tn", "l0_s5_bu_dw")
    gr['s5'] = (db_in, dc_out, da_row, dd)
    do, dgate, gr['onorm_g_a'] = _onorm_bwd(sv['o'], sv['gate'], p['onorm_g'], dya, "l0_onorm_bwd")
    dq, dk, dv, dgb = riders.run("l0_gdr_bwd", _gdr_bwd, sv['q'], sv['k'], sv['v'], sv['gb'], sv['tm'], sv['s'], do)
    conv = p['conv_qkv']
    dhq_q, dcw_q = _qkv_pre_bwd(sv['hq'], conv, dq, 0, 4, True, HEAD_A ** -0.5, "l0_q_pre_bwd")
    dhq_k, dcw_k = _qkv_pre_bwd(sv['hq'], conv, dk, 4, 4, True, 1.0, "l0_k_pre_bwd")
    dhq_v, dcw_v = _qkv_pre_bwd(sv['hq'], conv, dv, 8, 4, False, 1.0, "l0_v_pre_bwd")
    gr['conv_qkv_a'] = jnp.concatenate([dcw_q, dcw_k, dcw_v], axis=1)
    dhq = jnp.concatenate([dhq_q, dhq_k, dhq_v], axis=1)
    dba, da_log, ddt_bias = _gates_bwd(sv['ba'], p['arow'], p['brow'], dgb, "l0_gates_bwd")
    gr['a_log_a'], gr['dt_bias_a'] = da_log[:, 4:8], ddt_bias[:, 4:8]
    dxn = _mm(dhq, wts['w_qkv_t'], "nn", "l0_in_qkv_dx")
    dxn = _mm(dgate, wts['w_gate_t'], "nn", "l0_in_gate_dx", res=dxn)
    dxn = _mm(dba, wts['w_ba_t'], "nn", "l0_in_ba_dx", res=dxn)
    dxn = _mm(du, wts['w_u_t'], "nn", "l0_in_u_dx", res=dxn)
    dw_qkv_t = _mm(dhq, xn, "tn", "l0_in_qkv_dw", out_dtype=BF16)
    dw_gate_t = _mm(dgate, xn, "tn", "l0_in_gate_dw", out_dtype=BF16)
    dw_ba_t = _mm(dba, xn, "tn", "l0_in_ba_dw", out_dtype=BF16)
    dw_u_t = _mm(du, xn, "tn", "l0_in_u_dw", out_dtype=BF16)
    dw_in_t = _as_pieces(jnp.concatenate([dw_qkv_t, dw_gate_t, dw_ba_t[:8], dw_u_t], axis=0))
    riders.grad('w_in_t', jnp.concatenate(
        [dw_in_t, jnp.zeros((N_DEV, dict(PIECES)['w_in_t'] - W_IN_PIECE, D_MODEL), BF16)], axis=1))
    return dxn, gr


def _xa_fwd(x1, g, mem_n, wq, wkv_t, wo, tag, riders):
    xq = _rms_fwd(x1, g, BF16, tag + "_norm")
    q = _mm(xq, wq, "nn", tag + "_q", out_dtype=BF16)
    kv = _mm(mem_n, wkv_t, "nt", tag + "_kv", out_dtype=BF16)
    o = riders.run(tag + "_attn", _attn_fwd, q, kv)
    x2 = _mm(o, wo, "nn", tag + "_o", res=x1)
    return x2, dict(xq=xq, q=q, kv=kv, o=o)


def _xa_bwd(dx2, x1, g, mem_n, wq, wkv_t, wo, sv, tag, layer, riders):
    do = _mm(dx2, wo, "nt", tag + "_o_dx", out_dtype=BF16)
    riders.grad('wo%d' % layer, _as_pieces(_mm(sv['o'], dx2, "tn", tag + "_o_dw", out_dtype=BF16)))
    dq, dk, dv = _attn_bwd(sv['q'], sv['kv'], do, tag + "_attn_bwd")
    dkv = jnp.concatenate([dk, dv], axis=1).astype(BF16)
    dxq = _mm(dq, wq, "nt", tag + "_q_dx")
    riders.grad('wq%d' % layer, _as_pieces(_mm(sv['xq'], dq, "tn", tag + "_q_dw", out_dtype=BF16)))
    dmem_n = _mm(dkv, wkv_t, "nn", tag + "_kv_dx")
    riders.grad('wkv_t%d' % layer, _as_pieces(_mm(dkv, mem_n, "tn", tag + "_kv_dw", out_dtype=BF16)))
    dx1, dg = riders.run(tag + "_norm_bwd", _rms_bwd, x1, g, dxq, dx2)
    return dx1, dmem_n, dg


def _ffn_fwd(x2, g, w_up_t, conv, w_down, tag, riders):
    xf = _rms_fwd(x2, g, BF16, tag + "_norm")
    h = riders.run(tag + "_up", _mm, xf, w_up_t, "nt")
    a = riders.run(tag + "_act", _ffn_act_fwd, h, conv)
    x3 = _mm(a, w_down, "nn", tag + "_down", res=x2)
    return x3, dict(xf=xf, h=h, a=a)


def _ffn_bwd(dx3, x2, g, w_up_t, conv, w_down, sv, tag, layer, riders):
    da = _mm(dx3, w_down, "nt", tag + "_down_dx")
    riders.grad('down%d' % layer, _as_pieces(_mm(sv['a'], dx3, "tn", tag + "_down_dw", out_dtype=BF16)))
    dh, dconv = riders.run(tag + "_act_bwd", _ffn_act_bwd, sv['h'], conv, da)
    dxf = riders.run(tag + "_up_dx", _mm, dh, w_up_t, "nn")
    dw_up_t = riders.run(tag + "_up_dw", _mm, dh, sv['xf'], "tn", out_dtype=BF16)
    riders.grad('up_t%d' % layer, _as_pieces(dw_up_t))
    dx2, dg = riders.run(tag + "_norm_bwd", _rms_bwd, x2, g, dxf, dx3)
    return dx2, dconv, dg


BIG_NAMES, SMALL_NAMES, REP_NAMES = list(BIG_SHARDED), list(SMALL_SHARDED), list(REPLICATED)
BIG_SIZES = [int(np.prod(_shard_shape(*BIG_SHARDED[n]))) for n in BIG_NAMES]
SMALL_SIZES = [int(np.prod(_shard_shape(*SMALL_SHARDED[n]))) for n in SMALL_NAMES]


PIECES = [('w_in_t', 384), ('w_glu', 32), ('w_out', 128), ('pool_w', 32), ('wq0', 128), ('wq1', 128),
          ('wkv_t0', 256), ('wkv_t1', 256), ('wo0', 128), ('wo1', 128), ('up_t0', 704), ('up_t1', 704),
          ('down0', 352), ('down1', 352)]
PIECE_OFFS = dict(zip([k for k, _ in PIECES], np.concatenate([[0], np.cumsum([r for _, r in PIECES])[:-1]]).tolist()))
W_IN_ROWS = 4 * WIDTH_A + 2 * N_HEADS_A + SSM_WIDTH
W_IN_PIECE = W_IN_ROWS // N_DEV


def _row_tile(rows):
    return max(t for t in range(16, min(rows, 512) + 1, 16) if rows % t == 0)


class _Riders:
    def __init__(self):
        self.waiting = {}
        self.grads = {}
        self.groups = []
        self.reduced = {}

    def add(self, host, comm, then):
        self.waiting.setdefault(host, []).append((comm, then))

    def run(self, name, fn, *args, **kw):
        riders = self.waiting.pop(name, [])
        if not riders:
            return fn(*args, name=name, **kw)
        out, couts = fn(*args, name=name, comm=[c for c, _ in riders], **kw)
        for (_, then), got in zip(riders, couts):
            then(got)
        return out

    def exchange(self, comm, host, name, then):
        if host is None:
            then(_comm_only(comm, name))
        else:
            self.add(host, comm, then)

    def grad(self, key, pieces):
        self.grads[key] = pieces
        for group in [g for g in self.groups if all(k in self.grads for k in g[1])]:
            self.groups.remove(group)
            self._reduce(*group)

    def _reduce(self, name, keys, swap_host, chips_host):
        arrays = [self.grads[k] for k in keys]
        rows = sum(a.shape[1] for a in arrays)
        tile = _row_tile(rows)

        def after_chips(chip_sums, got):
            total = _chip_sum(chip_sums, got[0], name + "_chip_sum", tr=tile)
            off = 0
            for k, a in zip(keys, arrays):
                self.reduced[k] = total[off:off + a.shape[1]]
                off += a.shape[1]

        def after_swap(got):
            core = lax.axis_index("c")
            keep = jnp.concatenate(
                [lax.dynamic_index_in_dim(a.reshape(4, 2, a.shape[1], PACK_COLS), core, 1, keepdims=False)
                 for a in arrays], axis=1)
            chip_sums = _pair_sum(keep, got[0], name + "_pair_sum", tr=tile)
            self.exchange(_chips_comm(chip_sums), chips_host, name + "_to_chips",
                          functools.partial(after_chips, chip_sums))

        self.exchange(_swap_comm(arrays), swap_host, name + "_to_sibling", after_swap)


class _Weights:
    def __init__(self, inp):
        bf = lambda a: a.astype(BF16)
        local = {'w_in_t': bf(inp['w_in_ab'][0]).T, 'w_glu': bf(inp['w_glu_b'][0]), 'w_out': bf(inp['w_out_ab'][0]),
                 'pool_w': bf(inp['pool_w'][0]),
                 'small': _pack([inp[n] for n in SMALL_NAMES])}
        for l in range(2):
            local['wq%d' % l] = bf(inp['xa_wq'][l])
            local['wkv_t%d' % l] = bf(inp['xa_wkv'][l]).T
            local['wo%d' % l] = bf(inp['xa_wo'][l])
            local['up_t%d' % l] = bf(inp['ffn_w_up'][l]).T
            local['down%d' % l] = bf(inp['ffn_w_down'][l])
        self.local, self.full = local, {}

    def plan(self, keys):
        return _gather_comm([self.local[k] for k in keys])

    def land(self, keys, gathered):
        for k, g in zip(keys, gathered):
            if k == 'small':
                off = 0
                for n, size in zip(SMALL_NAMES, SMALL_SIZES):
                    self.full[n] = _merge_shards(g.reshape(N_DEV, -1)[:, off:off + size], *SMALL_SHARDED[n])
                    off += size
            elif k == 'pool_w':
                self.full[k] = jnp.swapaxes(g, 0, 1).reshape(len(POOL_WINDOWS), POOL_GROUP, POOL_GROUP)
            else:
                self.full[k] = g.reshape(N_DEV * g.shape[1], g.shape[2])


GATHER_FIRST = ['w_in_t', 'small']
GATHER_RIDES = [('l0_gdr_fwd', ['w_glu', 'w_out', 'wq0', 'wkv_t0', 'wo0', 'up_t0']),
                ('l0_s5_bu', ['pool_w', 'wq1']), ('l0_s5_scan', ['down0']), ('l0_s5_cx', ['wo1']),
                ('l0_xa_attn', ['wkv_t1']), ('l0_ffn_up', ['up_t1']), ('l0_ffn_act', ['down1'])]
GRAD_RIDES = [('g_down1', ['down1'], 'l1_ffn_act_bwd', 'l1_ffn_up_dx'),
              ('g_up1', ['up_t1'], 'l1_ffn_norm_bwd', 'l0_ffn_act_bwd'),
              ('g_xa1', ['wq1', 'wkv_t1', 'wo1', 'pool_w'], 'l1_mix_norm_bwd', 'l0_ffn_up_dx'),
              ('g_down0', ['down0'], 'l0_ffn_act_bwd', 'l0_ffn_up_dw'),
              ('g_up0', ['up_t0'], 'l0_ffn_norm_bwd', 'l0_gdr_bwd'),
              ('g_xa0', ['wq0', 'wkv_t0', 'wo0'], 'l0_xa_norm_bwd', 'l0_gdr_bwd'),
              ('g_out', ['w_out', 'w_glu'], 'l0_s5_cx_dx', 'l0_s5_scan_bwd'),
              ('g_in', ['w_in_t'], None, None)]


def _local_step(inp):
    f32_of = lambda n: inp[n].astype(F32)
    weights = _Weights(inp)
    riders = _Riders()
    riders.groups = list(GRAD_RIDES)
    full = weights.full
    weights.land(GATHER_FIRST, _comm_only(weights.plan(GATHER_FIRST), "gather_first"))
    for host, keys in GATHER_RIDES:
        riders.add(host, weights.plan(keys), functools.partial(weights.land, keys))
    w_in_t = full['w_in_t']
    wts0 = dict(w_qkv_t=w_in_t[:3 * WIDTH_A], w_gate_t=w_in_t[3 * WIDTH_A:4 * WIDTH_A],
                w_ba_t=jnp.concatenate([w_in_t[4 * WIDTH_A:4 * WIDTH_A + 8], jnp.zeros((LANE - 8, D_MODEL), BF16)], 0),
                w_u_t=w_in_t[4 * WIDTH_A + 8:])
    lb_disc, disc_vjp = jax.vjp(_s5_discretise, f32_of('ssm_lambda_re')[0], f32_of('ssm_lambda_im')[0],
                                f32_of('ssm_b_re')[0], f32_of('ssm_b_im')[0], f32_of('ssm_log_dt')[0])
    b_in, c_out, a_row = _s5_matrices(*lb_disc, f32_of('ssm_c_re')[0], f32_of('ssm_c_im')[0])
    zeros4 = jnp.zeros((1, 4), F32)
    p0 = dict(conv_qkv=full['conv_qkv_a'][0], onorm_g=f32_of('onorm_g_a'),
              arow=jnp.concatenate([zeros4, f32_of('a_log_a'), jnp.zeros((1, LANE - 8), F32)], 1),
              brow=jnp.concatenate([zeros4, f32_of('dt_bias_a'), jnp.zeros((1, LANE - 8), F32)], 1),
              b_in=b_in.astype(BF16), c_out=c_out.astype(BF16), a_row=a_row,
              d_row=f32_of('ssm_d').reshape(1, SSM_WIDTH), b_glu=f32_of('b_glu_b'))

    x0 = inp['x'][0]
    mem_n = _rms_fwd(inp['mem'][0], inp['norm_mem_g'], BF16, "mem_norm")
    xn0 = _rms_fwd(x0, inp['norm_mix_g'][0], BF16, "l0_mix_norm")
    x1, sv_mix0 = _hybrid_fwd(xn0, x0, wts0, p0, weights, riders)
    x2, sv_xa0 = _xa_fwd(x1, inp['norm_xa_g'][0], mem_n, full['wq0'], full['wkv_t0'], full['wo0'], "l0_xa", riders)
    x3, sv_ffn0 = _ffn_fwd(x2, inp['norm_ffn_g'][0], full['up_t0'], full['ffn_conv'][0], full['down0'], "l0_ffn", riders)
    xn1 = _rms_fwd(x3, inp['norm_mix_g'][1], F32, "l1_mix_norm")
    x4 = _pool_fwd(xn1, full['pool_w'], full['pool_scale'], x3, "l1_pool")
    x5, sv_xa1 = _xa_fwd(x4, inp['norm_xa_g'][1], mem_n, full['wq1'], full['wkv_t1'], full['wo1'], "l1_xa", riders)
    x6, sv_ffn1 = _ffn_fwd(x5, inp['norm_ffn_g'][1], full['up_t1'], full['ffn_conv'][1], full['down1'], "l1_ffn", riders)
    loss_part, dx6, dg_final = _loss_head(x6, inp['norm_final_g'], inp['loss_target'][0], "loss_head")

    dx5, dconv1, dg_ffn1 = _ffn_bwd(dx6, x5, inp['norm_ffn_g'][1], full['up_t1'], full['ffn_conv'][1], full['down1'],
                                    sv_ffn1, "l1_ffn", 1, riders)
    dx4, dmem1, dg_xa1 = _xa_bwd(dx5, x4, inp['norm_xa_g'][1], mem_n, full['wq1'], full['wkv_t1'], full['wo1'],
                                 sv_xa1, "l1_xa", 1, riders)
    dxn1, dpool_w, dpool_scale = _pool_bwd(xn1, full['pool_w'], full['pool_scale'], dx4, "l1_pool_bwd")
    pool_pieces = jnp.swapaxes(dpool_w.astype(BF16).reshape(len(POOL_WINDOWS), N_DEV, -1, POOL_GROUP), 0, 1)
    riders.grad('pool_w', pool_pieces.reshape(N_DEV, -1, PACK_COLS))
    dx3, dg_mix1 = riders.run("l1_mix_norm_bwd", _rms_bwd, x3, inp['norm_mix_g'][1], dxn1, dx4)
    dx2, dconv0, dg_ffn0 = _ffn_bwd(dx3, x2, inp['norm_ffn_g'][0], full['up_t0'], full['ffn_conv'][0], full['down0'],
                                    sv_ffn0, "l0_ffn", 0, riders)
    dx1, dmem0, dg_xa0 = _xa_bwd(dx2, x1, inp['norm_xa_g'][0], mem_n, full['wq0'], full['wkv_t0'], full['wo0'],
                                 sv_xa0, "l0_xa", 0, riders)
    dxn0, g_mix0 = _hybrid_bwd(dx1, xn0, wts0, p0, sv_mix0, riders)
    grad_x, dg_mix0 = _rms_bwd(x0, inp['norm_mix_g'][0], dxn0, dx1, "l0_mix_norm_bwd")
    _, dg_mem = _rms_bwd(inp['mem'][0], inp['norm_mem_g'], dmem0 + dmem1, None, "mem_norm_bwd")
    assert not riders.waiting and not riders.groups, (list(riders.waiting), riders.groups)

    db_in, dc_out, da_row, dd = g_mix0['s5']
    dlb_re, dlb_im, dbb_re, dbb_im, dc_re, dc_im = _s5_matrix_grads(db_in, dc_out, da_row)
    dlam_re, dlam_im, dbr, dbi, dlog_dt = disc_vjp((dlb_re, dlb_im, dbb_re, dbb_im))

    rep_grads = {
        'norm_mix_g': jnp.concatenate([dg_mix0, dg_mix1], 0), 'norm_xa_g': jnp.concatenate([dg_xa0, dg_xa1], 0),
        'norm_ffn_g': jnp.concatenate([dg_ffn0, dg_ffn1], 0), 'norm_mem_g': dg_mem.reshape(-1),
        'norm_final_g': dg_final.reshape(-1), 'a_log_a': g_mix0['a_log_a'], 'dt_bias_a': g_mix0['dt_bias_a'],
        'onorm_g_a': g_mix0['onorm_g_a'], 'ssm_lambda_re': dlam_re[None], 'ssm_lambda_im': dlam_im[None],
        'ssm_b_re': dbr[None], 'ssm_b_im': dbi[None], 'ssm_c_re': dc_re[None], 'ssm_c_im': dc_im[None],
        'ssm_d': dd.reshape(1, N_GROUPS, SSM_GROUP), 'ssm_log_dt': dlog_dt[None], 'b_glu_b': g_mix0['b_glu_b']}
    small_grads = {'conv_qkv_a': g_mix0['conv_qkv_a'][None], 'pool_scale': dpool_scale,
                   'ffn_conv': jnp.stack([dconv0, dconv1])}
    for key, rows in PIECES:
        assert riders.reduced[key].shape == (rows, PACK_COLS), key
    return loss_part, grad_x, riders.reduced, rep_grads, small_grads


def _reduce_small_gradients(rep_grads, small_grads):
    misc_local = _pack([rep_grads[n] for n in REP_NAMES] + [small_grads[n] for n in SMALL_NAMES])
    (misc_all,) = _comm_only(_gather_comm([misc_local]), "gather_small_grads")
    return _sum_leading(misc_all, "small_grads_sum")


def _update(inp, loss_part, grad_x, big_reduced, misc_sum):
    dev = _device_index()
    piece = lambda key, rows=None: big_reduced[key] if rows is None else big_reduced[key][:rows]
    both = lambda name, fn: jnp.stack([fn(piece(name + '0')), fn(piece(name + '1'))])
    ident, transpose = (lambda a: a), (lambda a: a.T)
    grads = {'w_in_ab': piece('w_in_t', W_IN_PIECE).T[None], 'w_glu_b': piece('w_glu').reshape(inp['w_glu_b'].shape),
             'w_out_ab': piece('w_out')[None], 'pool_w': piece('pool_w').reshape(inp['pool_w'].shape),
             'xa_wq': both('wq', ident), 'xa_wkv': both('wkv_t', transpose), 'xa_wo': both('wo', ident),
             'ffn_w_up': both('up_t', transpose), 'ffn_w_down': both('down', ident)}
    misc = _unpack(misc_sum, [inp[n].shape for n in REP_NAMES] + [SMALL_SHARDED[n][0] for n in SMALL_NAMES])
    for n, g in zip(REP_NAMES, misc):
        grads[n] = g
    for n, g in zip(SMALL_NAMES, misc[len(REP_NAMES):]):
        grads[n] = lax.dynamic_index_in_dim(_split_shards(g, SMALL_SHARDED[n][1]), dev, 0, keepdims=False
                                            ).reshape(inp[n].shape)
    upd = {}
    for n in BIG_NAMES:
        upd[n] = _adamw(inp[n], grads[n], inp['m_' + n], inp['v_' + n], "adamw_" + n)
    tiny_names = REP_NAMES + SMALL_NAMES
    rep_total = sum(int(np.prod(inp[n].shape)) for n in REP_NAMES)
    packs = [_pack([inp[prefix + n] for n in tiny_names]) for prefix in ('', 'm_', 'v_')]
    g_pack = _pack([misc_sum.reshape(-1)[:rep_total]] + [grads[n] for n in SMALL_NAMES])
    tiny_out = [_unpack(o, [inp[n].shape for n in tiny_names])
                for o in _adamw(packs[0], g_pack, packs[1], packs[2], "adamw_small")]
    for i, n in enumerate(tiny_names):
        upd[n] = tuple(o[i] for o in tiny_out)

    loss = lax.psum(loss_part[0, 0], ("x", "y", "c"))
    outs = [loss, grad_x[None]]
    outs += [grads[n] for n in WEIGHT_NAMES]
    for i in range(3):
        outs += [upd[n][i] for n in WEIGHT_NAMES]
    return tuple(outs)


def _step(inp):
    loss_part, grad_x, big_reduced, rep_grads, small_grads = _local_step(inp)
    misc_sum = _reduce_small_gradients(rep_grads, small_grads)
    return _update(inp, loss_part, grad_x, big_reduced, misc_sum)


INPUT_NAMES = (['x', 'mem'] + WEIGHT_NAMES + ['loss_target'] + ['m_' + n for n in WEIGHT_NAMES]
               + ['v_' + n for n in WEIGHT_NAMES])


def kernel(x, mem, norm_mix_g, norm_xa_g, norm_ffn_g, norm_mem_g, norm_final_g, w_in_ab, conv_qkv_a, a_log_a, dt_bias_a, onorm_g_a, ssm_lambda_re, ssm_lambda_im, ssm_b_re, ssm_b_im, ssm_c_re, ssm_c_im, ssm_d, ssm_log_dt, w_glu_b, b_glu_b, w_out_ab, pool_w, pool_scale, xa_wq, xa_wkv, xa_wo, ffn_w_up, ffn_conv, ffn_w_down, loss_target, m_norm_mix_g, m_norm_xa_g, m_norm_ffn_g, m_norm_mem_g, m_norm_final_g, m_w_in_ab, m_conv_qkv_a, m_a_log_a, m_dt_bias_a, m_onorm_g_a, m_ssm_lambda_re, m_ssm_lambda_im, m_ssm_b_re, m_ssm_b_im, m_ssm_c_re, m_ssm_c_im, m_ssm_d, m_ssm_log_dt, m_w_glu_b, m_b_glu_b, m_w_out_ab, m_pool_w, m_pool_scale, m_xa_wq, m_xa_wkv, m_xa_wo, m_ffn_w_up, m_ffn_conv, m_ffn_w_down, v_norm_mix_g, v_norm_xa_g, v_norm_ffn_g, v_norm_mem_g, v_norm_final_g, v_w_in_ab, v_conv_qkv_a, v_a_log_a, v_dt_bias_a, v_onorm_g_a, v_ssm_lambda_re, v_ssm_lambda_im, v_ssm_b_re, v_ssm_b_im, v_ssm_c_re, v_ssm_c_im, v_ssm_d, v_ssm_log_dt, v_w_glu_b, v_b_glu_b, v_w_out_ab, v_pool_w, v_pool_scale, v_xa_wq, v_xa_wkv, v_xa_wo, v_ffn_w_up, v_ffn_conv, v_ffn_w_down):
    args = (x, mem, norm_mix_g, norm_xa_g, norm_ffn_g, norm_mem_g, norm_final_g, w_in_ab, conv_qkv_a, a_log_a, dt_bias_a, onorm_g_a, ssm_lambda_re, ssm_lambda_im, ssm_b_re, ssm_b_im, ssm_c_re, ssm_c_im, ssm_d, ssm_log_dt, w_glu_b, b_glu_b, w_out_ab, pool_w, pool_scale, xa_wq, xa_wkv, xa_wo, ffn_w_up, ffn_conv, ffn_w_down, loss_target, m_norm_mix_g, m_norm_xa_g, m_norm_ffn_g, m_norm_mem_g, m_norm_final_g, m_w_in_ab, m_conv_qkv_a, m_a_log_a, m_dt_bias_a, m_onorm_g_a, m_ssm_lambda_re, m_ssm_lambda_im, m_ssm_b_re, m_ssm_b_im, m_ssm_c_re, m_ssm_c_im, m_ssm_d, m_ssm_log_dt, m_w_glu_b, m_b_glu_b, m_w_out_ab, m_pool_w, m_pool_scale, m_xa_wq, m_xa_wkv, m_xa_wo, m_ffn_w_up, m_ffn_conv, m_ffn_w_down, v_norm_mix_g, v_norm_xa_g, v_norm_ffn_g, v_norm_mem_g, v_norm_final_g, v_w_in_ab, v_conv_qkv_a, v_a_log_a, v_dt_bias_a, v_onorm_g_a, v_ssm_lambda_re, v_ssm_lambda_im, v_ssm_b_re, v_ssm_b_im, v_ssm_c_re, v_ssm_c_im, v_ssm_d, v_ssm_log_dt, v_w_glu_b, v_b_glu_b, v_w_out_ab, v_pool_w, v_pool_scale, v_xa_wq, v_xa_wkv, v_xa_wo, v_ffn_w_up, v_ffn_conv, v_ffn_w_down)
    return _step(dict(zip(INPUT_NAMES, args)))
```

```python
import functools
import math

import numpy as np
import jax
import jax.numpy as jnp
from jax import lax
from jax.experimental import pallas as pl
from jax.experimental.pallas import tpu as pltpu

F32, BF16 = jnp.float32, jnp.bfloat16
HIGH, HIGHEST = lax.Precision.HIGH, lax.Precision.HIGHEST
MESH = pl.DeviceIdType.MESH

N_DEV = 8
SEQ, D_MODEL, MEM_LEN = 2048, 1024, 256
WIDTH_A, N_HEADS_A, HEAD_A, CONV_A = 512, 4, 128, 4
GDR_CHUNK = 128
GDR_HEADS_PER_STEP = 4
SSM_WIDTH, SSM_GROUP, N_GROUPS, SSM_STATE = 512, 16, 32, 64
SSM_CH = N_GROUPS * SSM_STATE
SCAN_CB = 512
POOL_WINDOWS = (2, 4, 8, 16)
POOL_GROUP = 256
N_HEADS_X, HEAD_X = 4, 256
D_FF, CONV_FFN = 2816, 3
RMS_EPS = 1e-6
ADAM_LR, ADAM_B1, ADAM_B2, ADAM_EPS, ADAM_WD, ADAM_STEP = 0.001, 0.9, 0.999, 1e-08, 0.01, 10
LANE = 128
PACK_COLS = 1024
VMEM_LIMIT_BYTES = 56 * 1024 * 1024


def _params(sem=None):
    return pltpu.CompilerParams(dimension_semantics=sem, vmem_limit_bytes=VMEM_LIMIT_BYTES)


class Comm:
    def __init__(self, inputs, out_shapes, sems, start, end, mid=None):
        self.inputs, self.out_shapes, self.sems = list(inputs), list(out_shapes), list(sems)
        self.start, self.mid, self.end = start, mid, end


def _merge_comms(comms):
    comms = [c for c in comms if c is not None]
    if not comms:
        return None, []
    bounds, ni, no, ns = [], 0, 0, 0
    for c in comms:
        bounds.append((ni, no, ns))
        ni, no, ns = ni + len(c.inputs), no + len(c.out_shapes), ns + len(c.sems)

    def phase(which):
        def run(ins, outs, sems):
            for c, (i0, o0, s0) in zip(comms, bounds):
                fn = getattr(c, which)
                if fn is not None:
                    fn(ins[i0:i0 + len(c.inputs)], outs[o0:o0 + len(c.out_shapes)], sems[s0:s0 + len(c.sems)])
        return run

    merged = Comm([a for c in comms for a in c.inputs], [s for c in comms for s in c.out_shapes],
                  [s for c in comms for s in c.sems], phase("start"), phase("end"), phase("mid"))
    return merged, [(o0, o0 + len(c.out_shapes)) for c, (_, o0, _) in zip(comms, bounds)]


def _call(body, *, name, grid, in_specs, out_specs, out_shape, args, scratch_shapes=(), sem=None, comm=None):
    single = not isinstance(out_shape, (list, tuple))
    out_specs_l = [out_specs] if single else list(out_specs)
    out_shape_l = [out_shape] if single else list(out_shape)
    scratch_shapes = list(scratch_shapes)
    merged, spans = _merge_comms(comm if isinstance(comm, (list, tuple)) else [comm])
    if merged is None:
        outs = pl.pallas_call(body, name=name, grid=grid, in_specs=list(in_specs), out_specs=out_specs_l,
                              out_shape=out_shape_l, scratch_shapes=scratch_shapes, compiler_params=_params(sem))(*args)
        outs = outs[0] if single else outs
        return outs if comm is None else (outs, [])
    n_in, n_out, n_scr = len(in_specs), len(out_specs_l), len(scratch_shapes)
    ci, co = len(merged.inputs), len(merged.out_shapes)
    total = int(np.prod(grid))

    def wrapped(*refs):
        ins, cins = refs[:n_in], refs[n_in:n_in + ci]
        outs, couts = refs[n_in + ci:n_in + ci + n_out], refs[n_in + ci + n_out:n_in + ci + n_out + co]
        scr, csems = refs[n_in + ci + n_out + co:n_in + ci + n_out + co + n_scr], refs[n_in + ci + n_out + co + n_scr:]
        lin = pl.program_id(0)
        for d in range(1, len(grid)):
            lin = lin * grid[d] + pl.program_id(d)
        pl.when(lin == 0)(lambda: merged.start(cins, couts, csems))
        body(*ins, *outs, *scr)
        def finish():
            merged.mid(cins, couts, csems)
            merged.end(cins, couts, csems)

        pl.when(lin == total - 1)(finish)

    any_spec = pl.BlockSpec(memory_space=pl.ANY)
    res = pl.pallas_call(
        wrapped, name=name, grid=grid, in_specs=list(in_specs) + [any_spec] * ci,
        out_specs=out_specs_l + [any_spec] * co, out_shape=out_shape_l + merged.out_shapes,
        scratch_shapes=scratch_shapes + merged.sems,
        compiler_params=_params(("arbitrary",) * len(grid)))(*args, *merged.inputs)
    outs, couts = res[:n_out], res[n_out:]
    return (outs[0] if single else list(outs)), [list(couts[a:b]) for a, b in spans]


def _comm_only(comm, name):
    def body():
        pass

    _, couts = _call(body, name=name, grid=(1,), in_specs=[], out_specs=[], out_shape=[], args=[], comm=comm)
    return couts[0]


def _tile(dim, pref):
    best = None
    for t in range(LANE, min(dim, pref) + 1, LANE):
        if dim % t == 0:
            best = t
    return best if best is not None else dim


MM_VMEM_BUDGET = 40 * 1024 * 1024


def _mm_tiles(m, n, k, a_bytes, b_bytes, o_bytes, r_bytes):
    for tk in (k, _tile(k, 2048), _tile(k, 1024), _tile(k, 512)):
        for tm, tn in ((1024, 1536), (1024, 1024), (1024, 512), (512, 512), (256, 512), (256, 256)):
            tm, tn = _tile(m, tm), _tile(n, tn)
            acc = 0 if tk == k else tm * tn * 4
            need = 2 * (tm * tk * a_bytes + tk * tn * b_bytes + tm * tn * (o_bytes + r_bytes)) + acc
            if need <= MM_VMEM_BUDGET:
                return tm, tn, tk
    raise ValueError("no matmul tiling fits VMEM")


def _mm(a, b, mode, name, out_dtype=F32, res=None, comm=None):
    if mode == "nn":
        (m, k), n = a.shape, b.shape[1]
    elif mode == "nt":
        (m, k), n = a.shape, b.shape[0]
    else:
        (k, m), n = a.shape, b.shape[1]
    tm, tn, tk = _mm_tiles(m, n, k, a.dtype.itemsize, b.dtype.itemsize, jnp.dtype(out_dtype).itemsize,
                           0 if res is None else res.dtype.itemsize)
    nk = k // tk
    dims = {"nn": ((1,), (0,)), "nt": ((1,), (1,)), "tn": ((0,), (0,))}[mode]

    def body(*refs):
        if res is None:
            a_ref, b_ref, o_ref = refs[:3]
            r_ref = None
        else:
            a_ref, b_ref, r_ref, o_ref = refs[:4]
        part = lax.dot_general(a_ref[...].astype(BF16), b_ref[...].astype(BF16), (dims, ((), ())),
                               preferred_element_type=F32)

        def finish(out):
            if r_ref is not None:
                out = out + r_ref[...].astype(F32)
            o_ref[...] = out.astype(out_dtype)

        if nk == 1:
            finish(part)
            return
        acc = refs[-1]
        kk = pl.program_id(2)

        @pl.when(kk == 0)
        def _():
            acc[...] = part

        @pl.when(kk > 0)
        def _():
            acc[...] += part

        @pl.when(kk == nk - 1)
        def _():
            finish(acc[...])

    a_spec = (pl.BlockSpec((tk, tm), lambda i, j, q: (q, i)) if mode == "tn"
              else pl.BlockSpec((tm, tk), lambda i, j, q: (i, q)))
    b_spec = (pl.BlockSpec((tn, tk), lambda i, j, q: (j, q)) if mode == "nt"
              else pl.BlockSpec((tk, tn), lambda i, j, q: (q, j)))
    o_spec = pl.BlockSpec((tm, tn), lambda i, j, q: (i, j))
    in_specs, args = [a_spec, b_spec], [a, b]
    if res is not None:
        in_specs.append(o_spec)
        args.append(res)
    return _call(body, name=name, grid=(m // tm, n // tn, nk), in_specs=in_specs, out_specs=o_spec,
                 out_shape=jax.ShapeDtypeStruct((m, n), out_dtype),
                 scratch_shapes=[] if nk == 1 else [pltpu.VMEM((tm, tn), F32)],
                 sem=("parallel", "parallel", "arbitrary"), args=args, comm=comm)


def _mm_bd(a, b, mode, name, out_dtype=F32, res=None, comm=None, tm=1024):
    if mode == "tn":
        k = a.shape[0]
        nb = min(a.shape[1], b.shape[1]) // LANE
        ma, n = a.shape[1] // nb, b.shape[1] // nb

        def body(a_ref, b_ref, o_ref):
            o_ref[0] = lax.dot_general(a_ref[...].astype(BF16), b_ref[...].astype(BF16), (((0,), (0,)), ((), ())),
                                       preferred_element_type=F32).astype(out_dtype)

        return _call(body, name=name, grid=(nb,),
                     in_specs=[pl.BlockSpec((k, ma), lambda j: (0, j)), pl.BlockSpec((k, n), lambda j: (0, j))],
                     out_specs=pl.BlockSpec((1, ma, n), lambda j: (j, 0, 0)),
                     out_shape=jax.ShapeDtypeStruct((nb, ma, n), out_dtype), sem=("parallel",), args=(a, b), comm=comm)
    m = a.shape[0]
    nb = b.shape[0]
    ka = a.shape[1] // nb
    n = b.shape[2] if mode == "nn" else b.shape[1]
    tm = _tile(m, tm)
    dims = ((1,), (0,)) if mode == "nn" else ((1,), (1,))

    def body(*refs):
        if res is None:
            a_ref, b_ref, o_ref = refs
            r_ref = None
        else:
            a_ref, b_ref, r_ref, o_ref = refs
        out = lax.dot_general(a_ref[...].astype(BF16), b_ref[0].astype(BF16), (dims, ((), ())),
                              preferred_element_type=F32)
        if r_ref is not None:
            out = out + r_ref[...].astype(F32)
        o_ref[...] = out.astype(out_dtype)

    o_spec = pl.BlockSpec((tm, n), lambda i, j: (i, j))
    in_specs = [pl.BlockSpec((tm, ka), lambda i, j: (i, j)), pl.BlockSpec((1,) + b.shape[1:], lambda i, j: (j, 0, 0))]
    args = [a, b]
    if res is not None:
        in_specs.append(o_spec)
        args.append(res)
    return _call(body, name=name, grid=(m // tm, nb), in_specs=in_specs, out_specs=o_spec,
                 out_shape=jax.ShapeDtypeStruct((m, nb * n), out_dtype), sem=("parallel", "parallel"),
                 args=args, comm=comm)


def _rms_fwd(x, g, out_dtype, name, tr=256):
    rows, d = x.shape

    def body(x_ref, g_ref, o_ref):
        xv = x_ref[...]
        r = lax.rsqrt(jnp.mean(xv * xv, axis=-1, keepdims=True) + RMS_EPS)
        o_ref[...] = (xv * r * g_ref[...]).astype(out_dtype)

    return pl.pallas_call(
        body, name=name, grid=(rows // tr,),
        in_specs=[pl.BlockSpec((tr, d), lambda i: (i, 0)), pl.BlockSpec((1, d), lambda i: (0, 0))],
        out_specs=pl.BlockSpec((tr, d), lambda i: (i, 0)), out_shape=jax.ShapeDtypeStruct((rows, d), out_dtype),
        compiler_params=_params(("parallel",)))(x, g.reshape(1, d))


def _rms_bwd(x, g, dy, dres, name, tr=256, comm=None):
    rows, d = x.shape

    def body(*refs):
        if dres is None:
            x_ref, g_ref, dy_ref, dx_ref, dg_ref = refs
            r_ref = None
        else:
            x_ref, g_ref, dy_ref, r_ref, dx_ref, dg_ref = refs

        @pl.when(pl.program_id(0) == 0)
        def _():
            dg_ref[...] = jnp.zeros_like(dg_ref)

        xv, dyv = x_ref[...], dy_ref[...].astype(F32)
        r = lax.rsqrt(jnp.mean(xv * xv, axis=-1, keepdims=True) + RMS_EPS)
        xh = xv * r
        dyg = dyv * g_ref[...]
        dx = r * (dyg - xh * jnp.mean(dyg * xh, axis=-1, keepdims=True))
        if r_ref is not None:
            dx = dx + r_ref[...]
        dx_ref[...] = dx
        dg_ref[...] += jnp.sum(dyv * xh, axis=0, keepdims=True)

    blk = pl.BlockSpec((tr, d), lambda i: (i, 0))
    vec = pl.BlockSpec((1, d), lambda i: (0, 0))
    in_specs, args = [blk, vec, blk], [x, g.reshape(1, d), dy]
    if dres is not None:
        in_specs.append(blk)
        args.append(dres)
    return _call(
        body, name=name, grid=(rows // tr,), in_specs=in_specs, out_specs=[blk, vec],
        out_shape=[jax.ShapeDtypeStruct((rows, d), F32), jax.ShapeDtypeStruct((1, d), F32)],
        sem=("arbitrary",), args=args, comm=comm)


def _loss_head(x, g, target, name, tr=256):
    rows, d = x.shape

    def body(x_ref, g_ref, t_ref, loss_ref, dx_ref, dg_ref):
        @pl.when(pl.program_id(0) == 0)
        def _():
            dg_ref[...] = jnp.zeros_like(dg_ref)
            loss_ref[...] = jnp.zeros_like(loss_ref)

        xv = x_ref[...]
        r = lax.rsqrt(jnp.mean(xv * xv, axis=-1, keepdims=True) + RMS_EPS)
        xh = xv * r
        err = xh * g_ref[...] - t_ref[...]
        loss_ref[...] += 0.5 * jnp.sum(jnp.mean(err * err, axis=-1, keepdims=True), keepdims=True)
        dyv = err * (1.0 / d)
        dyg = dyv * g_ref[...]
        dx_ref[...] = r * (dyg - xh * jnp.mean(dyg * xh, axis=-1, keepdims=True))
        dg_ref[...] += jnp.sum(dyv * xh, axis=0, keepdims=True)

    blk = pl.BlockSpec((tr, d), lambda i: (i, 0))
    vec = pl.BlockSpec((1, d), lambda i: (0, 0))
    return pl.pallas_call(
        body, name=name, grid=(rows // tr,), in_specs=[blk, vec, blk],
        out_specs=[pl.BlockSpec((1, 1), lambda i: (0, 0)), blk, vec],
        out_shape=[jax.ShapeDtypeStruct((1, 1), F32), jax.ShapeDtypeStruct((rows, d), F32),
                   jax.ShapeDtypeStruct((1, d), F32)],
        compiler_params=_params(("arbitrary",)))(x, g.reshape(1, d), target)


def _shift_down(x, s):
    rows = lax.broadcasted_iota(jnp.int32, x.shape, 0)
    return jnp.where(rows >= s, pltpu.roll(x, s, 0), 0.0)


def _shift_up(x, s):
    n = x.shape[0]
    rows = lax.broadcasted_iota(jnp.int32, x.shape, 0)
    return jnp.where(rows < n - s, pltpu.roll(x, n - s, 0), 0.0)


def _sigmoid(x):
    return 1.0 / (1.0 + jnp.exp(-x))


def _silu_and_grad(x):
    s = _sigmoid(x)
    return x * s, s * (1.0 + x * (1.0 - s))


_GELU_C0, _GELU_C1 = math.sqrt(2.0 / math.pi), 0.044715


def _gelu_and_grad(x):
    th = jnp.tanh(_GELU_C0 * (x + _GELU_C1 * x * x * x))
    y = 0.5 * x * (1.0 + th)
    dy = 0.5 * (1.0 + th) + 0.5 * x * (1.0 - th * th) * _GELU_C0 * (1.0 + 3.0 * _GELU_C1 * x * x)
    return y, dy


def _ffn_act_fwd(h, w, name, tc=256, comm=None):
    t = h.shape[0]
    nb = D_FF // tc

    def body(hg_ref, hv_ref, wg_ref, wv_ref, a_ref):
        def conv(x, wr):
            return wr[2:3, :] * x + wr[1:2, :] * _shift_down(x, 1) + wr[0:1, :] * _shift_down(x, 2)

        cg = conv(hg_ref[...], wg_ref[...])
        cv = conv(hv_ref[...], wv_ref[...])
        a_ref[...] = (cg * _sigmoid(cg) * cv).astype(BF16)

    return _call(
        body, name=name, grid=(nb,),
        in_specs=[pl.BlockSpec((t, tc), lambda j: (0, j)), pl.BlockSpec((t, tc), lambda j: (0, j + nb)),
                  pl.BlockSpec((CONV_FFN, tc), lambda j: (0, j)), pl.BlockSpec((CONV_FFN, tc), lambda j: (0, j + nb))],
        out_specs=pl.BlockSpec((t, tc), lambda j: (0, j)), out_shape=jax.ShapeDtypeStruct((t, D_FF), BF16),
        sem=("parallel",), args=(h, h, w, w), comm=comm)


def _ffn_act_bwd(h, w, da, name, tc=256, comm=None):
    t = h.shape[0]
    nb = D_FF // tc

    def body(hg_ref, hv_ref, wg_ref, wv_ref, da_ref, dhg_ref, dhv_ref, dwg_ref, dwv_ref):
        hg, hv, wg, wv = hg_ref[...], hv_ref[...], wg_ref[...], wv_ref[...]
        hg1, hg2, hv1, hv2 = _shift_down(hg, 1), _shift_down(hg, 2), _shift_down(hv, 1), _shift_down(hv, 2)
        cg = wg[2:3, :] * hg + wg[1:2, :] * hg1 + wg[0:1, :] * hg2
        cv = wv[2:3, :] * hv + wv[1:2, :] * hv1 + wv[0:1, :] * hv2
        sg, dsg = _silu_and_grad(cg)
        dav = da_ref[...].astype(F32)
        dcv = dav * sg
        dcg = dav * cv * dsg

        def conv_t(dc, wr):
            return wr[2:3, :] * dc + wr[1:2, :] * _shift_up(dc, 1) + wr[0:1, :] * _shift_up(dc, 2)

        dhg_ref[...] = conv_t(dcg, wg).astype(BF16)
        dhv_ref[...] = conv_t(dcv, wv).astype(BF16)
        dwg_ref[0:1, :] = jnp.sum(dcg * hg2, axis=0, keepdims=True)
        dwg_ref[1:2, :] = jnp.sum(dcg * hg1, axis=0, keepdims=True)
        dwg_ref[2:3, :] = jnp.sum(dcg * hg, axis=0, keepdims=True)
        dwv_ref[0:1, :] = jnp.sum(dcv * hv2, axis=0, keepdims=True)
        dwv_ref[1:2, :] = jnp.sum(dcv * hv1, axis=0, keepdims=True)
        dwv_ref[2:3, :] = jnp.sum(dcv * hv, axis=0, keepdims=True)

    big = lambda off: pl.BlockSpec((t, tc), lambda j: (0, j + off))
    small = lambda off: pl.BlockSpec((CONV_FFN, tc), lambda j: (0, j + off))
    res = _call(
        body, name=name, grid=(nb,),
        in_specs=[big(0), big(nb), small(0), small(nb), big(0)],
        out_specs=[big(0), big(0), small(0), small(0)],
        out_shape=[jax.ShapeDtypeStruct((t, D_FF), BF16), jax.ShapeDtypeStruct((t, D_FF), BF16),
                   jax.ShapeDtypeStruct((CONV_FFN, D_FF), F32), jax.ShapeDtypeStruct((CONV_FFN, D_FF), F32)],
        sem=("parallel",), args=(h, h, w, w, da), comm=comm)
    (dhg, dhv, dwg, dwv), couts = res if comm is not None else (res, None)
    out = (jnp.concatenate([dhg, dhv], axis=1), jnp.concatenate([dwg, dwv], axis=1))
    return out if comm is None else (out, couts)


def _attn_probs(q, k):
    s = lax.dot_general(q.astype(BF16), k.astype(BF16), (((1,), (1,)), ((), ())),
                        preferred_element_type=F32) * (HEAD_X ** -0.5)
    s = s - jnp.max(s, axis=-1, keepdims=True)
    p = jnp.exp(s)
    return p / jnp.sum(p, axis=-1, keepdims=True)


def _attn_fwd(q, kv, name, tq=512, comm=None):
    t = q.shape[0]

    def body(q_ref, k_ref, v_ref, o_ref):
        p = _attn_probs(q_ref[...], k_ref[...])
        o_ref[...] = jnp.dot(p.astype(BF16), v_ref[...].astype(BF16), preferred_element_type=F32).astype(BF16)

    return _call(
        body, name=name, grid=(N_HEADS_X, t // tq),
        in_specs=[pl.BlockSpec((tq, HEAD_X), lambda h, i: (i, h)),
                  pl.BlockSpec((MEM_LEN, HEAD_X), lambda h, i: (0, h)),
                  pl.BlockSpec((MEM_LEN, HEAD_X), lambda h, i: (0, h + N_HEADS_X))],
        out_specs=pl.BlockSpec((tq, HEAD_X), lambda h, i: (i, h)),
        out_shape=jax.ShapeDtypeStruct((t, N_HEADS_X * HEAD_X), BF16),
        sem=("parallel", "parallel"), args=(q, kv, kv), comm=comm)


def _attn_bwd(q, kv, do, name, tq=512):
    t = q.shape[0]

    def body(q_ref, k_ref, v_ref, do_ref, dq_ref, dk_ref, dv_ref):
        @pl.when(pl.program_id(1) == 0)
        def _():
            dk_ref[...] = jnp.zeros_like(dk_ref)
            dv_ref[...] = jnp.zeros_like(dv_ref)

        qb, kb, vb, dob = (r[...].astype(BF16) for r in (q_ref, k_ref, v_ref, do_ref))
        p = _attn_probs(qb, kb)
        dp = lax.dot_general(dob, vb, (((1,), (1,)), ((), ())), preferred_element_type=F32)
        ds = p * (dp - jnp.sum(dp * p, axis=-1, keepdims=True)) * (HEAD_X ** -0.5)
        dsb = ds.astype(BF16)
        dq_ref[...] = jnp.dot(dsb, kb, preferred_element_type=F32).astype(BF16)
        dk_ref[...] += lax.dot_general(dsb, qb, (((0,), (0,)), ((), ())), preferred_element_type=F32)
        dv_ref[...] += lax.dot_general(p.astype(BF16), dob, (((0,), (0,)), ((), ())), preferred_element_type=F32)

    qs = pl.BlockSpec((tq, HEAD_X), lambda h, i: (i, h))
    ms = pl.BlockSpec((MEM_LEN, HEAD_X), lambda h, i: (0, h))
    return pl.pallas_call(
        body, name=name, grid=(N_HEADS_X, t // tq),
        in_specs=[qs, ms, pl.BlockSpec((MEM_LEN, HEAD_X), lambda h, i: (0, h + N_HEADS_X)), qs],
        out_specs=[qs, ms, ms],
        out_shape=[jax.ShapeDtypeStruct((t, D_MODEL), BF16), jax.ShapeDtypeStruct((MEM_LEN, D_MODEL), F32),
                   jax.ShapeDtypeStruct((MEM_LEN, D_MODEL), F32)],
        compiler_params=_params(("parallel", "arbitrary")))(q, kv, kv, do)


def _pool_counts(t, win):
    pos = lax.broadcasted_iota(jnp.int32, (t, 1), 0).astype(F32) + 1.0
    return 1.0 / jnp.minimum(pos, float(win))


def _pool_delta(xv, win):
    s, step = xv, 1
    while step < win:
        s = s + _shift_down(s, step)
        step *= 2
    return s * _pool_counts(xv.shape[0], win) - xv


def _pool_delta_t(dv, win):
    s, step = dv * _pool_counts(dv.shape[0], win), 1
    while step < win:
        s = s + _shift_up(s, step)
        step *= 2
    return s - dv


def _pool_fwd(xn, w, scale, res, name):
    t = xn.shape[0]

    def make_branch(win, xn_ref, w_ref, s_ref, r_ref, o_ref):
        def branch():
            dl = _pool_delta(xn_ref[...], win)
            y = jnp.dot(dl.astype(BF16), w_ref[0], preferred_element_type=F32)
            o_ref[...] = r_ref[...] + y * s_ref[...]
        return branch

    def body(xn_ref, w_ref, s_ref, r_ref, o_ref):
        for gi, win in enumerate(POOL_WINDOWS):
            pl.when(pl.program_id(0) == gi)(make_branch(win, xn_ref, w_ref, s_ref, r_ref, o_ref))

    blk = pl.BlockSpec((t, POOL_GROUP), lambda g: (0, g))
    return pl.pallas_call(
        body, name=name, grid=(len(POOL_WINDOWS),),
        in_specs=[blk, pl.BlockSpec((1, POOL_GROUP, POOL_GROUP), lambda g: (g, 0, 0)),
                  pl.BlockSpec((1, POOL_GROUP), lambda g: (0, g)), blk],
        out_specs=blk, out_shape=jax.ShapeDtypeStruct((t, D_MODEL), F32),
        compiler_params=_params(("parallel",)))(xn, w, scale, res)


def _pool_bwd(xn, w, scale, dmix, name):
    t = xn.shape[0]

    def make_branch(win, xn_ref, w_ref, s_ref, d_ref, dxn_ref, dw_ref, ds_ref):
        def branch():
            dl = _pool_delta(xn_ref[...], win).astype(BF16)
            wv = w_ref[0]
            dm = d_ref[...]
            y = jnp.dot(dl, wv, preferred_element_type=F32)
            ds_ref[...] = jnp.sum(dm * y, axis=0, keepdims=True)
            dy = (dm * s_ref[...]).astype(BF16)
            dw_ref[0] = lax.dot_general(dl, dy, (((0,), (0,)), ((), ())), preferred_element_type=F32)
            ddl = lax.dot_general(dy, wv, (((1,), (1,)), ((), ())), preferred_element_type=F32)
            dxn_ref[...] = _pool_delta_t(ddl, win)
        return branch

    def body(*refs):
        for gi, win in enumerate(POOL_WINDOWS):
            pl.when(pl.program_id(0) == gi)(make_branch(win, *refs))

    blk = pl.BlockSpec((t, POOL_GROUP), lambda g: (0, g))
    wspec = pl.BlockSpec((1, POOL_GROUP, POOL_GROUP), lambda g: (g, 0, 0))
    vec = pl.BlockSpec((1, POOL_GROUP), lambda g: (0, g))
    return pl.pallas_call(
        body, name=name, grid=(len(POOL_WINDOWS),), in_specs=[blk, wspec, vec, blk], out_specs=[blk, wspec, vec],
        out_shape=[jax.ShapeDtypeStruct((t, D_MODEL), F32),
                   jax.ShapeDtypeStruct((len(POOL_WINDOWS), POOL_GROUP, POOL_GROUP), F32),
                   jax.ShapeDtypeStruct((1, D_MODEL), F32)],
        compiler_params=_params(("parallel",)))(xn, w, scale, dmix)


def _qkv_conv(h, wr):
    return (wr[3:4, :] * h + wr[2:3, :] * _shift_down(h, 1) + wr[1:2, :] * _shift_down(h, 2)
            + wr[0:1, :] * _shift_down(h, 3))


def _qkv_pre_fwd(h, w, col0, ncols, normalize, scale, name):
    t = h.shape[0]

    def body(h_ref, w_ref, o_ref):
        c = _qkv_conv(h_ref[...], w_ref[...])
        s = c * _sigmoid(c)
        if normalize:
            s = s * lax.rsqrt(jnp.sum(s * s, axis=-1, keepdims=True) + 1e-6) * scale
        o_ref[...] = s

    return pl.pallas_call(
        body, name=name, grid=(ncols,),
        in_specs=[pl.BlockSpec((t, HEAD_A), lambda j: (0, j + col0)), pl.BlockSpec((CONV_A, HEAD_A), lambda j: (0, j + col0))],
        out_specs=pl.BlockSpec((t, HEAD_A), lambda j: (0, j)), out_shape=jax.ShapeDtypeStruct((t, ncols * HEAD_A), F32),
        compiler_params=_params(("parallel",)))(h, w)


def _qkv_pre_bwd(h, w, dy, col0, ncols, normalize, scale, name):
    t = h.shape[0]

    def body(h_ref, w_ref, dy_ref, dh_ref, dw_ref):
        hv, wr, dyv = h_ref[...], w_ref[...], dy_ref[...]
        h1, h2, h3 = _shift_down(hv, 1), _shift_down(hv, 2), _shift_down(hv, 3)
        c = wr[3:4, :] * hv + wr[2:3, :] * h1 + wr[1:2, :] * h2 + wr[0:1, :] * h3
        s, dsilu = _silu_and_grad(c)
        if normalize:
            r = lax.rsqrt(jnp.sum(s * s, axis=-1, keepdims=True) + 1e-6)
            y = s * r
            dyv = dyv * scale
            ds = r * (dyv - y * jnp.sum(dyv * y, axis=-1, keepdims=True))
        else:
            ds = dyv
        dc = ds * dsilu
        dh = (wr[3:4, :] * dc + wr[2:3, :] * _shift_up(dc, 1) + wr[1:2, :] * _shift_up(dc, 2)
              + wr[0:1, :] * _shift_up(dc, 3))
        dh_ref[...] = dh.astype(BF16)
        dw_ref[0:1, :] = jnp.sum(dc * h3, axis=0, keepdims=True)
        dw_ref[1:2, :] = jnp.sum(dc * h2, axis=0, keepdims=True)
        dw_ref[2:3, :] = jnp.sum(dc * h1, axis=0, keepdims=True)
        dw_ref[3:4, :] = jnp.sum(dc * hv, axis=0, keepdims=True)

    return pl.pallas_call(
        body, name=name, grid=(ncols,),
        in_specs=[pl.BlockSpec((t, HEAD_A), lambda j: (0, j + col0)), pl.BlockSpec((CONV_A, HEAD_A), lambda j: (0, j + col0)),
                  pl.BlockSpec((t, HEAD_A), lambda j: (0, j))],
        out_specs=[pl.BlockSpec((t, HEAD_A), lambda j: (0, j)), pl.BlockSpec((CONV_A, HEAD_A), lambda j: (0, j))],
        out_shape=[jax.ShapeDtypeStruct((t, ncols * HEAD_A), BF16), jax.ShapeDtypeStruct((CONV_A, ncols * HEAD_A), F32)],
        compiler_params=_params(("parallel",)))(h, w, dy)


def _softplus(x):
    return jnp.maximum(x, 0.0) + jnp.log1p(jnp.exp(-jnp.abs(x)))


def _gates_fwd(ba, arow, brow, name):
    t = ba.shape[0]

    def body(x_ref, a_ref, b_ref, o_ref):
        xv = x_ref[...]
        lane = lax.broadcasted_iota(jnp.int32, xv.shape, 1)
        beta = _sigmoid(xv)
        g = -jnp.exp(a_ref[...]) * _softplus(xv + b_ref[...])
        o_ref[...] = jnp.where(lane < N_HEADS_A, beta, jnp.where(lane < 2 * N_HEADS_A, g, 0.0))

    return pl.pallas_call(body, name=name, out_shape=jax.ShapeDtypeStruct((t, LANE), F32),
                          compiler_params=_params())(ba, arow, brow)


def _gates_bwd(ba, arow, brow, dgb, name):
    t = ba.shape[0]

    def body(x_ref, a_ref, b_ref, d_ref, dx_ref, da_ref, db_ref):
        xv = x_ref[...]
        dv = d_ref[0] + d_ref[1] + d_ref[2] + d_ref[3]
        lane = lax.broadcasted_iota(jnp.int32, xv.shape, 1)
        beta = _sigmoid(xv)
        ea = jnp.exp(a_ref[...])
        z = xv + b_ref[...]
        dgv = jnp.where((lane >= N_HEADS_A) & (lane < 2 * N_HEADS_A), dv, 0.0) * (-ea)
        dz = dgv * _sigmoid(z)
        dx = jnp.where(lane < N_HEADS_A, dv * beta * (1.0 - beta), dz)
        dx_ref[...] = dx.astype(BF16)
        db_ref[...] = jnp.sum(dz, axis=0, keepdims=True)
        da_ref[...] = jnp.sum(dgv * _softplus(z), axis=0, keepdims=True)

    return pl.pallas_call(
        body, name=name,
        out_shape=[jax.ShapeDtypeStruct((t, LANE), BF16), jax.ShapeDtypeStruct((1, LANE), F32),
                   jax.ShapeDtypeStruct((1, LANE), F32)],
        compiler_params=_params())(ba, arow, brow, dgb)


def _dot(a, b, prec=None):
    if prec is None:
        return jnp.dot(a.astype(BF16), b.astype(BF16), preferred_element_type=F32)
    return jnp.dot(a, b, precision=prec, preferred_element_type=F32)


def _dot_nt(a, b, prec=None):
    if prec is None:
        a, b = a.astype(BF16), b.astype(BF16)
    return lax.dot_general(a, b, (((1,), (1,)), ((), ())), precision=prec, preferred_element_type=F32)


def _dot_tn(a, b, prec=None):
    if prec is None:
        a, b = a.astype(BF16), b.astype(BF16)
    return lax.dot_general(a, b, (((0,), (0,)), ((), ())), precision=prec, preferred_element_type=F32)


def _gdr_chunk_terms(k, beta, g):
    c = GDR_CHUNK
    row = lax.broadcasted_iota(jnp.int32, (c, c), 0)
    col = lax.broadcasted_iota(jnp.int32, (c, c), 1)
    causal, strict = row >= col, row > col
    gcum = _dot(causal.astype(F32), jnp.broadcast_to(g, (c, c)), HIGHEST)
    diff = gcum - gcum.T
    decay = jnp.where(causal, jnp.exp(jnp.where(causal, diff, 0.0)), 0.0)
    kb = k * beta
    kk = _dot_nt(kb, k)
    return row, col, causal, strict, gcum, decay, kb, kk


def _unit_lower_inverse(a):
    c = a.shape[0]
    eye = (lax.broadcasted_iota(jnp.int32, (c, c), 0) == lax.broadcasted_iota(jnp.int32, (c, c), 1)).astype(F32)
    p = -a
    inv = eye + p
    step = 1
    while 2 * step < c:
        p = _dot(p, p, HIGH)
        inv = inv + _dot(inv, p, HIGH)
        step *= 2
    return inv


def _head_gates(gates, head):
    lane = lax.broadcasted_iota(jnp.int32, gates.shape, 1)
    beta = jnp.sum(jnp.where(lane == head, gates, 0.0), axis=1, keepdims=True)
    g = jnp.sum(jnp.where(lane == head + N_HEADS_A, gates, 0.0), axis=1, keepdims=True)
    return beta, g


def _gdr_fwd(q, k, v, gates, name, comm=None):
    t = q.shape[0]
    c = GDR_CHUNK
    n = t // c

    hps = GDR_HEADS_PER_STEP

    def one_head(hh, q_ref, k_ref, v_ref, gb_ref, o_ref, tm_ref, s_ref, state):
        cols = slice(hh * HEAD_A, (hh + 1) * HEAD_A)
        qv, kv, vv = q_ref[:, cols], k_ref[:, cols], v_ref[:, cols]
        beta, g = _head_gates(gb_ref[...], pl.program_id(0) * hps + hh)
        row, col, causal, strict, gcum, decay, kb, kk = _gdr_chunk_terms(kv, beta, g)
        tm = _unit_lower_inverse(jnp.where(strict, kk * decay, 0.0))
        e = jnp.exp(gcum)
        u = _dot(tm, vv * beta, HIGH)
        w = _dot(tm, kb * e, HIGH)
        p = jnp.where(causal, _dot_nt(qv, kv) * decay, 0.0)
        s = state[hh]
        s_ref[hh, 0] = s
        tm_ref[hh, 0] = tm
        vn = u - _dot(w, s)
        o_ref[:, cols] = _dot(qv * e, s) + _dot(p, vn)
        glast = gcum[c - 1:c, :]
        state[hh] = s * jnp.exp(glast) + _dot_tn(kv * jnp.exp(glast - gcum), vn)

    def body(*refs):
        state = refs[-1]

        @pl.when(pl.program_id(1) == 0)
        def _():
            state[...] = jnp.zeros_like(state)

        for hh in range(hps):
            one_head(hh, *refs)

    blk = pl.BlockSpec((c, hps * HEAD_A), lambda h, i: (i, h))
    mat = pl.BlockSpec((hps, 1, c, c), lambda h, i: (h, i, 0, 0))
    return _call(
        body, name=name, grid=(N_HEADS_A // hps, n),
        in_specs=[blk, blk, blk, pl.BlockSpec((c, LANE), lambda h, i: (i, 0))],
        out_specs=[blk, mat, mat],
        out_shape=[jax.ShapeDtypeStruct((t, WIDTH_A), F32), jax.ShapeDtypeStruct((N_HEADS_A, n, c, c), F32),
                   jax.ShapeDtypeStruct((N_HEADS_A, n, HEAD_A, HEAD_A), F32)],
        scratch_shapes=[pltpu.VMEM((hps, HEAD_A, HEAD_A), F32)], sem=("parallel", "arbitrary"),
        args=(q, k, v, gates), comm=comm)


def _gdr_bwd(q, k, v, gates, tm_all, s_all, do, name, comm=None):
    t = q.shape[0]
    c = GDR_CHUNK
    n = t // c

    hps = GDR_HEADS_PER_STEP

    def one_head(hh, q_ref, k_ref, v_ref, gb_ref, tm_ref, s_ref, do_ref, dq_ref, dk_ref, dv_ref, dgb_ref, dstate):
        cols = slice(hh * HEAD_A, (hh + 1) * HEAD_A)
        qv, kv, vv, dov = q_ref[:, cols], k_ref[:, cols], v_ref[:, cols], do_ref[:, cols]
        head = pl.program_id(0) * hps + hh
        beta, g = _head_gates(gb_ref[...], head)
        tm, s, dsp = tm_ref[hh, 0], s_ref[hh, 0], dstate[hh]
        row, col, causal, strict, gcum, decay, kb, kk = _gdr_chunk_terms(kv, beta, g)
        e = jnp.exp(gcum)
        vb, kbe = vv * beta, kb * e
        u = _dot(tm, vb, HIGH)
        w = _dot(tm, kbe, HIGH)
        qk = _dot_nt(qv, kv)
        p = jnp.where(causal, qk * decay, 0.0)
        vn = u - _dot(w, s)
        glast = gcum[c - 1:c, :]
        el = jnp.exp(glast)
        f = jnp.exp(glast - gcum)
        kd = kv * f
        qe = qv * e

        dvn = _dot_tn(p, dov) + _dot(kd, dsp)
        dglast = el[:, 0:1] * jnp.sum(s * dsp, keepdims=True)
        dkd = _dot_nt(vn, dsp)
        dk = dkd * f
        df = jnp.sum(dkd * kv, axis=1, keepdims=True) * f[:, 0:1]
        dglast = dglast + jnp.sum(df, keepdims=True)
        dgc = -df
        dp = jnp.where(causal, _dot_nt(dov, vn), 0.0)
        dqe = _dot_nt(dov, s)
        dq = dqe * e
        de = jnp.sum(dqe * qv, axis=1, keepdims=True)
        dstate[hh] = dsp * el + _dot_tn(qe, dov) - _dot_tn(w, dvn)
        dw = -_dot_nt(dvn, s)
        dvb = _dot_tn(tm, dvn, HIGH)
        dkbe = _dot_tn(tm, dw, HIGH)
        da = -jnp.where(strict, _dot_nt(dvb, u) + _dot_nt(dkbe, w), 0.0)
        dkk = da * decay
        dqk = dp * decay
        dd = da * kk + dp * qk
        dq = dq + _dot(dqk, kv)
        dk = dk + _dot_tn(dqk, qv)
        dkb = _dot(dkk, kv) + dkbe * e
        dk = dk + _dot_tn(dkk, kb)
        de = de + jnp.sum(dkbe * kb, axis=1, keepdims=True)
        dk = dk + dkb * beta
        dbeta = jnp.sum(dkb * kv, axis=1, keepdims=True) + jnp.sum(dvb * vv, axis=1, keepdims=True)
        m = dd * decay
        dgc = dgc + jnp.sum(m, axis=1, keepdims=True) - jnp.sum(m.T, axis=1, keepdims=True)
        dgc = dgc + de * e[:, 0:1]
        dgc = dgc + jnp.where(row[:, 0:1] == c - 1, dglast, 0.0)
        dg = _dot((row <= col).astype(F32), jnp.broadcast_to(dgc, (c, c)), HIGHEST)
        dq_ref[:, cols] = dq
        dk_ref[:, cols] = dk
        dv_ref[:, cols] = dvb * beta
        lane = lax.broadcasted_iota(jnp.int32, (c, LANE), 1)
        dgb_ref[hh] = jnp.where(lane == head, dbeta, jnp.where(lane == head + N_HEADS_A, dg, 0.0))

    def body(*refs):
        dstate = refs[-1]

        @pl.when(pl.program_id(1) == 0)
        def _():
            dstate[...] = jnp.zeros_like(dstate)

        for hh in range(hps):
            one_head(hh, *refs)

    blk = pl.BlockSpec((c, hps * HEAD_A), lambda h, i: (n - 1 - i, h))
    mat = pl.BlockSpec((hps, 1, c, c), lambda h, i: (h, n - 1 - i, 0, 0))
    return _call(
        body, name=name, grid=(N_HEADS_A // hps, n),
        in_specs=[blk, blk, blk, pl.BlockSpec((c, LANE), lambda h, i: (n - 1 - i, 0)), mat, mat, blk],
        out_specs=[blk, blk, blk, pl.BlockSpec((hps, c, LANE), lambda h, i: (h, n - 1 - i, 0))],
        out_shape=[jax.ShapeDtypeStruct((t, WIDTH_A), F32)] * 3 + [jax.ShapeDtypeStruct((N_HEADS_A, t, LANE), F32)],
        scratch_shapes=[pltpu.VMEM((hps, HEAD_A, HEAD_A), F32)], sem=("parallel", "arbitrary"),
        args=(q, k, v, gates, tm_all, s_all, do), comm=comm)


_B_NN, _B_NT, _B_TN = ((2,), (1,)), ((2,), (2,)), ((1,), (1,))


def _bdot(a, b, dims=_B_NN, prec=None):
    if prec is None:
        a, b = a.astype(BF16), b.astype(BF16)
    return lax.dot_general(a, b, (dims, ((0,), (0,))), precision=prec, preferred_element_type=F32)


def _heads_of(ref):
    return jnp.stack([ref[:, h * HEAD_A:(h + 1) * HEAD_A] for h in range(N_HEADS_A)])


def _all_head_gates(gates):
    pairs = [_head_gates(gates, h) for h in range(N_HEADS_A)]
    return jnp.stack([b for b, _ in pairs]), jnp.stack([g for _, g in pairs])


def _gdr_terms(k, beta, g):
    h, c = k.shape[0], GDR_CHUNK
    row = lax.broadcasted_iota(jnp.int32, (c, c), 0)
    col = lax.broadcasted_iota(jnp.int32, (c, c), 1)
    causal, strict = row >= col, row > col
    lower = jnp.broadcast_to(causal.astype(F32), (h, c, c))
    gcum = _bdot(lower, jnp.broadcast_to(g, (h, c, c)), prec=HIGHEST)
    diff = gcum - jnp.swapaxes(gcum, 1, 2)
    decay = jnp.where(causal, jnp.exp(jnp.where(causal, diff, 0.0)), 0.0)
    kb = k * beta
    return row, col, causal, strict, gcum, decay, kb, _bdot(kb, k, _B_NT)


def _unit_lower_inverses(a):
    c = a.shape[1]
    eye = (lax.broadcasted_iota(jnp.int32, (c, c), 0) == lax.broadcasted_iota(jnp.int32, (c, c), 1)).astype(F32)
    p = -a
    inv = eye + p
    step = 1
    while 2 * step < c:
        p = _bdot(p, p, prec=HIGH)
        inv = inv + _bdot(inv, p, prec=HIGH)
        step *= 2
    return inv


def _gdr_fwd(q, k, v, gates, name, comm=None):
    t = q.shape[0]
    c, nh = GDR_CHUNK, N_HEADS_A
    n = t // c

    def body(q_ref, k_ref, v_ref, gb_ref, o_ref, tm_ref, s_ref, state):
        @pl.when(pl.program_id(0) == 0)
        def _():
            state[...] = jnp.zeros_like(state)

        qv, kv, vv = _heads_of(q_ref), _heads_of(k_ref), _heads_of(v_ref)
        beta, g = _all_head_gates(gb_ref[...])
        row, col, causal, strict, gcum, decay, kb, kk = _gdr_terms(kv, beta, g)
        tm = _unit_lower_inverses(jnp.where(strict, kk * decay, 0.0))
        e = jnp.exp(gcum)
        u = _bdot(tm, vv * beta, prec=HIGH)
        w = _bdot(tm, kb * e, prec=HIGH)
        p = jnp.where(causal, _bdot(qv, kv, _B_NT) * decay, 0.0)
        s = state[...]
        s_ref[:, 0] = s
        tm_ref[:, 0] = tm
        vn = u - _bdot(w, s)
        o = _bdot(qv * e, s) + _bdot(p, vn)
        for h in range(nh):
            o_ref[:, h * HEAD_A:(h + 1) * HEAD_A] = o[h]
        glast = gcum[:, c - 1:c, :]
        state[...] = s * jnp.exp(glast) + _bdot(kv * jnp.exp(glast - gcum), vn, _B_TN)

    blk = pl.BlockSpec((c, WIDTH_A), lambda i: (i, 0))
    mat = pl.BlockSpec((nh, 1, c, c), lambda i: (0, i, 0, 0))
    return _call(
        body, name=name, grid=(n,), in_specs=[blk, blk, blk, pl.BlockSpec((c, LANE), lambda i: (i, 0))],
        out_specs=[blk, mat, mat],
        out_shape=[jax.ShapeDtypeStruct((t, WIDTH_A), F32), jax.ShapeDtypeStruct((nh, n, c, c), F32),
                   jax.ShapeDtypeStruct((nh, n, HEAD_A, HEAD_A), F32)],
        scratch_shapes=[pltpu.VMEM((nh, HEAD_A, HEAD_A), F32)], sem=("arbitrary",),
        args=(q, k, v, gates), comm=comm)


def _gdr_bwd(q, k, v, gates, tm_all, s_all, do, name, comm=None):
    t = q.shape[0]
    c, nh = GDR_CHUNK, N_HEADS_A
    n = t // c

    def body(q_ref, k_ref, v_ref, gb_ref, tm_ref, s_ref, do_ref, dq_ref, dk_ref, dv_ref, dgb_ref, dstate):
        @pl.when(pl.program_id(0) == 0)
        def _():
            dstate[...] = jnp.zeros_like(dstate)

        qv, kv, vv, dov = _heads_of(q_ref), _heads_of(k_ref), _heads_of(v_ref), _heads_of(do_ref)
        beta, g = _all_head_gates(gb_ref[...])
        tm, s, dsp = tm_ref[:, 0], s_ref[:, 0], dstate[...]
        row, col, causal, strict, gcum, decay, kb, kk = _gdr_terms(kv, beta, g)
        rowsum = lambda x: jnp.sum(x, axis=2, keepdims=True)
        e = jnp.exp(gcum)
        vb, kbe = vv * beta, kb * e
        u = _bdot(tm, vb, prec=HIGH)
        w = _bdot(tm, kbe, prec=HIGH)
        qk = _bdot(qv, kv, _B_NT)
        p = jnp.where(causal, qk * decay, 0.0)
        vn = u - _bdot(w, s)
        glast = gcum[:, c - 1:c, :]
        el = jnp.exp(glast)
        f = jnp.exp(glast - gcum)
        kd = kv * f
        qe = qv * e

        dvn = _bdot(p, dov, _B_TN) + _bdot(kd, dsp)
        dglast = el[:, :, 0:1] * jnp.sum(s * dsp, axis=(1, 2), keepdims=True)
        dkd = _bdot(vn, dsp, _B_NT)
        dk = dkd * f
        df = rowsum(dkd * kv) * f[:, :, 0:1]
        dglast = dglast + jnp.sum(df, axis=1, keepdims=True)
        dgc = -df
        dp = jnp.where(causal, _bdot(dov, vn, _B_NT), 0.0)
        dqe = _bdot(dov, s, _B_NT)
        dq = dqe * e
        de = rowsum(dqe * qv)
        dstate[...] = dsp * el + _bdot(qe, dov, _B_TN) - _bdot(w, dvn, _B_TN)
        dw = -_bdot(dvn, s, _B_NT)
        dvb = _bdot(tm, dvn, _B_TN, prec=HIGH)
        dkbe = _bdot(tm, dw, _B_TN, prec=HIGH)
        da = -jnp.where(strict, _bdot(dvb, u, _B_NT) + _bdot(dkbe, w, _B_NT), 0.0)
        dkk = da * decay
        dqk = dp * decay
        dd = da * kk + dp * qk
        dq = dq + _bdot(dqk, kv)
        dk = dk + _bdot(dqk, qv, _B_TN)
        dkb = _bdot(dkk, kv) + dkbe * e
        dk = dk + _bdot(dkk, kb, _B_TN)
        de = de + rowsum(dkbe * kb)
        dk = dk + dkb * beta
        dbeta = rowsum(dkb * kv) + rowsum(dvb * vv)
        m = dd * decay
        dgc = dgc + rowsum(m) - rowsum(jnp.swapaxes(m, 1, 2))
        dgc = dgc + de * e[:, :, 0:1]
        dgc = dgc + jnp.where(row[:, 0:1] == c - 1, dglast, 0.0)
        upper = jnp.broadcast_to((row <= col).astype(F32), (nh, c, c))
        dg = _bdot(upper, jnp.broadcast_to(dgc, (nh, c, c)), prec=HIGHEST)
        dv = dvb * beta
        for h in range(nh):
            cols = slice(h * HEAD_A, (h + 1) * HEAD_A)
            dq_ref[:, cols] = dq[h]
            dk_ref[:, cols] = dk[h]
            dv_ref[:, cols] = dv[h]
        head = lax.broadcasted_iota(jnp.int32, (nh, c, LANE), 0)
        lane = lax.broadcasted_iota(jnp.int32, (nh, c, LANE), 2)
        dgb_ref[...] = jnp.where(lane == head, dbeta, jnp.where(lane == head + nh, dg, 0.0))

    blk = pl.BlockSpec((c, WIDTH_A), lambda i: (n - 1 - i, 0))
    mat = pl.BlockSpec((nh, 1, c, c), lambda i: (0, n - 1 - i, 0, 0))
    return _call(
        body, name=name, grid=(n,),
        in_specs=[blk, blk, blk, pl.BlockSpec((c, LANE), lambda i: (n - 1 - i, 0)), mat, mat, blk],
        out_specs=[blk, blk, blk, pl.BlockSpec((nh, c, LANE), lambda i: (0, n - 1 - i, 0))],
        out_shape=[jax.ShapeDtypeStruct((t, WIDTH_A), F32)] * 3 + [jax.ShapeDtypeStruct((nh, t, LANE), F32)],
        scratch_shapes=[pltpu.VMEM((nh, HEAD_A, HEAD_A), F32)], sem=("arbitrary",),
        args=(q, k, v, gates, tm_all, s_all, do), comm=comm)


def _onorm_fwd(o, gate, g, name):
    t = o.shape[0]

    def body(o_ref, gate_ref, g_ref, y_ref):
        ov, gv = o_ref[...], gate_ref[...]
        r = lax.rsqrt(jnp.mean(ov * ov, axis=-1, keepdims=True) + RMS_EPS)
        y_ref[...] = (ov * r * g_ref[...] * gv * _sigmoid(gv)).astype(BF16)

    blk = pl.BlockSpec((t, HEAD_A), lambda j: (0, j))
    return pl.pallas_call(
        body, name=name, grid=(N_HEADS_A,), in_specs=[blk, blk, pl.BlockSpec((1, HEAD_A), lambda j: (0, 0))],
        out_specs=blk, out_shape=jax.ShapeDtypeStruct((t, WIDTH_A), BF16),
        compiler_params=_params(("parallel",)))(o, gate, g)


def _onorm_bwd(o, gate, g, dy, name):
    t = o.shape[0]

    def body(o_ref, gate_ref, g_ref, dy_ref, do_ref, dgate_ref, dg_ref):
        @pl.when(pl.program_id(0) == 0)
        def _():
            dg_ref[...] = jnp.zeros_like(dg_ref)

        ov, gv, dyv = o_ref[...], gate_ref[...], dy_ref[...].astype(F32)
        r = lax.rsqrt(jnp.mean(ov * ov, axis=-1, keepdims=True) + RMS_EPS)
        oh = ov * r
        sg, dsg = _silu_and_grad(gv)
        dgate_ref[...] = (dyv * oh * g_ref[...] * dsg).astype(BF16)
        dn = dyv * sg
        dg_ref[...] += jnp.sum(dn * oh, axis=0, keepdims=True)
        dng = dn * g_ref[...]
        do_ref[...] = r * (dng - oh * jnp.mean(dng * oh, axis=-1, keepdims=True))

    blk = pl.BlockSpec((t, HEAD_A), lambda j: (0, j))
    vec = pl.BlockSpec((1, HEAD_A), lambda j: (0, 0))
    return pl.pallas_call(
        body, name=name, grid=(N_HEADS_A,), in_specs=[blk, blk, vec, blk], out_specs=[blk, blk, vec],
        out_shape=[jax.ShapeDtypeStruct((t, WIDTH_A), F32), jax.ShapeDtypeStruct((t, WIDTH_A), BF16),
                   jax.ShapeDtypeStruct((1, HEAD_A), F32)],
        compiler_params=_params(("arbitrary",)))(o, gate, g, dy)


def _cmul(ar, ai, br, bi):
    return ar * br - ai * bi, ar * bi + ai * br


def _scan_tables(ar, ai, reverse):
    p1 = (ar, ai)
    p2 = _cmul(*p1, *p1)
    p4 = _cmul(*p2, *p2)
    p8 = _cmul(*p4, *p4)
    p3 = _cmul(*p2, *p1)
    p5 = _cmul(*p4, *p1)
    p6 = _cmul(*p4, *p2)
    p7 = _cmul(*p4, *p3)
    pows = [p1, p2, p3, p4, p5, p6, p7, p8]
    rows = lax.broadcasted_iota(jnp.int32, (8, ar.shape[1]), 0)
    tr = jnp.zeros((8, ar.shape[1]), F32)
    ti = jnp.zeros((8, ar.shape[1]), F32)
    for r in range(8):
        pw = pows[7 - r] if reverse else pows[r]
        tr = jnp.where(rows == r, pw[0], tr)
        ti = jnp.where(rows == r, pw[1], ti)
    return p1, p2, p4, p8, tr, ti


def _tile_scan(xr, xi, p1, p2, p4, reverse):
    rows = lax.broadcasted_iota(jnp.int32, xr.shape, 0)
    for s, (pr, pi) in ((1, p1), (2, p2), (4, p4)):
        if reverse:
            keep = rows < 8 - s
            sr, si = pltpu.roll(xr, 8 - s, 0), pltpu.roll(xi, 8 - s, 0)
        else:
            keep = rows >= s
            sr, si = pltpu.roll(xr, s, 0), pltpu.roll(xi, s, 0)
        sr, si = jnp.where(keep, sr, 0.0), jnp.where(keep, si, 0.0)
        mr, mi = _cmul(pr, pi, sr, si)
        xr, xi = xr + mr, xi + mi
    return xr, xi


def _s5_scan_fwd(bu, a, name, tb=512, comm=None):
    t = bu.shape[0]
    cb = SCAN_CB
    nt = t // tb

    def body(b_ref, a_ref, x_ref, carry):
        @pl.when(pl.program_id(1) == 0)
        def _():
            carry[...] = jnp.zeros_like(carry)

        ar, ai = a_ref[:, 0:cb], a_ref[:, cb:2 * cb]
        p1, p2, p4, p8, tr, ti = _scan_tables(ar, ai, False)

        def step(j, c):
            cr, ci = c
            i = pl.multiple_of(j * 8, 8)
            xr, xi = _tile_scan(b_ref[pl.ds(i, 8), 0:cb], b_ref[pl.ds(i, 8), cb:2 * cb], p1, p2, p4, False)
            mr, mi = _cmul(tr, ti, cr, ci)
            xr, xi = xr + mr, xi + mi
            x_ref[pl.ds(i, 8), 0:cb] = xr
            x_ref[pl.ds(i, 8), cb:2 * cb] = xi
            return xr[7:8, :], xi[7:8, :]

        cr, ci = lax.fori_loop(0, tb // 8, step, (carry[0:1, :], carry[1:2, :]), unroll=2)
        carry[0:1, :] = cr
        carry[1:2, :] = ci

    blk = pl.BlockSpec((tb, 2 * cb), lambda j, i: (i, j))
    return _call(
        body, name=name, grid=(SSM_CH // cb, nt),
        in_specs=[blk, pl.BlockSpec((1, 2 * cb), lambda j, i: (0, j))], out_specs=blk,
        out_shape=jax.ShapeDtypeStruct((t, 2 * SSM_CH), F32), scratch_shapes=[pltpu.VMEM((8, cb), F32)],
        sem=("parallel", "arbitrary"), args=(bu, a), comm=comm)


def _s5_scan_bwd(dx, x, a, name, tb=512, comm=None):
    t = dx.shape[0]
    cb = SCAN_CB
    nt = t // tb
    nj = tb // 8

    def body(d_ref, x_ref, xp_ref, a_ref, l_ref, da_ref, carry, acc):
        tblk = pl.program_id(1)

        @pl.when(tblk == 0)
        def _():
            carry[...] = jnp.zeros_like(carry)
            acc[...] = jnp.zeros_like(acc)

        ar, ai = a_ref[:, 0:cb], a_ref[:, cb:2 * cb]
        p1, p2, p4, p8, tr, ti = _scan_tables(ar, -ai, True)
        rows = lax.broadcasted_iota(jnp.int32, (8, cb), 0)

        def step(jj, c):
            cr, ci, sr_acc, si_acc = c
            j = nj - 1 - jj
            i = pl.multiple_of(j * 8, 8)
            lr, li = _tile_scan(d_ref[pl.ds(i, 8), 0:cb], d_ref[pl.ds(i, 8), cb:2 * cb], p1, p2, p4, True)
            mr, mi = _cmul(tr, ti, cr, ci)
            lr, li = lr + mr, li + mi
            l_ref[pl.ds(i, 8), 0:cb] = lr
            l_ref[pl.ds(i, 8), cb:2 * cb] = li
            ip = pl.multiple_of(jnp.maximum(j - 1, 0) * 8, 8)
            prev_r = jnp.where(j > 0, x_ref[pl.ds(ip, 8), 0:cb], xp_ref[:, 0:cb])
            prev_i = jnp.where(j > 0, x_ref[pl.ds(ip, 8), cb:2 * cb], xp_ref[:, cb:2 * cb])
            edge = jnp.where(jnp.logical_and(j == 0, tblk == nt - 1), 0.0, 1.0)
            xs_r = jnp.where(rows == 0, pltpu.roll(prev_r, 1, 0) * edge, pltpu.roll(x_ref[pl.ds(i, 8), 0:cb], 1, 0))
            xs_i = jnp.where(rows == 0, pltpu.roll(prev_i, 1, 0) * edge, pltpu.roll(x_ref[pl.ds(i, 8), cb:2 * cb], 1, 0))
            sr_acc = sr_acc + lr * xs_r + li * xs_i
            si_acc = si_acc + li * xs_r - lr * xs_i
            return lr[0:1, :], li[0:1, :], sr_acc, si_acc

        cr, ci, sr_acc, si_acc = lax.fori_loop(
            0, nj, step, (carry[0:1, :], carry[1:2, :], acc[:, 0:cb], acc[:, cb:2 * cb]))
        carry[0:1, :] = cr
        carry[1:2, :] = ci
        acc[:, 0:cb] = sr_acc
        acc[:, cb:2 * cb] = si_acc

        @pl.when(tblk == nt - 1)
        def _():
            da_ref[...] = jnp.sum(acc[...], axis=0, keepdims=True)

    blk = pl.BlockSpec((tb, 2 * cb), lambda j, i: (nt - 1 - i, j))
    prev = pl.BlockSpec((8, 2 * cb), lambda j, i: (jnp.maximum((nt - 1 - i) * (tb // 8) - 1, 0), j))
    vec = pl.BlockSpec((1, 2 * cb), lambda j, i: (0, j))
    return _call(
        body, name=name, grid=(SSM_CH // cb, nt), in_specs=[blk, blk, prev, vec], out_specs=[blk, vec],
        out_shape=[jax.ShapeDtypeStruct((t, 2 * SSM_CH), F32), jax.ShapeDtypeStruct((1, 2 * SSM_CH), F32)],
        scratch_shapes=[pltpu.VMEM((8, cb), F32), pltpu.VMEM((8, 2 * cb), F32)],
        sem=("parallel", "arbitrary"), args=(dx, x, x, a), comm=comm)


def _glu_fwd(yc, u, dvec, wg, bg, name, tr=256):
    t = yc.shape[0]

    def body(yc_ref, u_ref, d_ref, w_ref, b_ref, yl_ref, yb_ref):
        yl = yc_ref[...] + d_ref[...] * u_ref[...]
        yl_ref[...] = yl
        yg, _ = _gelu_and_grad(yl)
        z = jnp.dot(yg.astype(BF16), w_ref[...], preferred_element_type=F32) + b_ref[...]
        yb_ref[...] = (yg * _sigmoid(z)).astype(BF16)

    blk = pl.BlockSpec((tr, SSM_WIDTH), lambda i: (i, 0))
    vec = pl.BlockSpec((1, SSM_WIDTH), lambda i: (0, 0))
    return pl.pallas_call(
        body, name=name, grid=(t // tr,),
        in_specs=[blk, blk, vec, pl.BlockSpec((SSM_WIDTH, SSM_WIDTH), lambda i: (0, 0)), vec],
        out_specs=[blk, blk],
        out_shape=[jax.ShapeDtypeStruct((t, SSM_WIDTH), F32), jax.ShapeDtypeStruct((t, SSM_WIDTH), BF16)],
        compiler_params=_params(("parallel",)))(yc, u, dvec, wg, bg)


def _glu_bwd(yl, u, dvec, wg, bg, dyb, name, tr=256):
    t = yl.shape[0]

    def body(yl_ref, u_ref, d_ref, w_ref, b_ref, dy_ref, dyl_ref, du_ref, dw_ref, db_ref, dd_ref):
        @pl.when(pl.program_id(0) == 0)
        def _():
            dw_ref[...] = jnp.zeros_like(dw_ref)
            db_ref[...] = jnp.zeros_like(db_ref)
            dd_ref[...] = jnp.zeros_like(dd_ref)

        ylv, dyv, wv = yl_ref[...], dy_ref[...].astype(F32), w_ref[...]
        yg, dgelu = _gelu_and_grad(ylv)
        ygb = yg.astype(BF16)
        z = jnp.dot(ygb, wv, preferred_element_type=F32) + b_ref[...]
        sg = _sigmoid(z)
        dz = dyv * yg * sg * (1.0 - sg)
        dzb = dz.astype(BF16)
        dyg = dyv * sg + lax.dot_general(dzb, wv, (((1,), (1,)), ((), ())), preferred_element_type=F32)
        dyl = dyg * dgelu
        dyl_ref[...] = dyl.astype(BF16)
        du_ref[...] = dyl * d_ref[...]
        dw_ref[...] += lax.dot_general(ygb, dzb, (((0,), (0,)), ((), ())), preferred_element_type=F32)
        db_ref[...] += jnp.sum(dz, axis=0, keepdims=True)
        dd_ref[...] += jnp.sum(dyl * u_ref[...], axis=0, keepdims=True)

    blk = pl.BlockSpec((tr, SSM_WIDTH), lambda i: (i, 0))
    vec = pl.BlockSpec((1, SSM_WIDTH), lambda i: (0, 0))
    wsp = pl.BlockSpec((SSM_WIDTH, SSM_WIDTH), lambda i: (0, 0))
    return pl.pallas_call(
        body, name=name, grid=(t // tr,), in_specs=[blk, blk, vec, wsp, vec, blk],
        out_specs=[blk, blk, wsp, vec, vec],
        out_shape=[jax.ShapeDtypeStruct((t, SSM_WIDTH), BF16), jax.ShapeDtypeStruct((t, SSM_WIDTH), F32),
                   jax.ShapeDtypeStruct((SSM_WIDTH, SSM_WIDTH), F32), jax.ShapeDtypeStruct((1, SSM_WIDTH), F32),
                   jax.ShapeDtypeStruct((1, SSM_WIDTH), F32)],
        compiler_params=_params(("arbitrary",)))(yl, u, dvec, wg, bg, dyb)


def _mesh_pos():
    return lax.axis_index("x"), lax.axis_index("y"), lax.axis_index("c")


def _device_index():
    x, y, c = _mesh_pos()
    return 4 * x + 2 * y + c


def _gather_comm(arrays):
    na = len(arrays)

    def own_copy(ins, outs, sems, ai):
        return pltpu.make_async_copy(ins[ai], outs[ai].at[_device_index()], sems[2].at[ai])

    def ctx(ins, outs, sems):
        send_sems, recv_sems = sems[:2]
        x, y, c = _mesh_pos()
        chips = [(1 - x, y), (x, 1 - y), (1 - x, 1 - y)]

        def copy(ai, kk, block, to, own=False):
            slot = outs[ai].at[4 * block[0] + 2 * block[1] + block[2]]
            return pltpu.make_async_remote_copy(
                src_ref=ins[ai] if own else slot, dst_ref=slot, send_sem=send_sems.at[ai, kk],
                recv_sem=recv_sems.at[ai, kk], device_id=to, device_id_type=MESH)

        return (x, y, c), (x, y, 1 - c), chips, c, copy

    def start(ins, outs, sems):
        me, sibling, chips, c, copy = ctx(ins, outs, sems)
        for ai in range(na):
            copy(ai, 0, me, sibling, own=True).start()
            for j, chip in enumerate(chips):
                copy(ai, 1 + j, me, (*chip, c), own=True).start()
        for ai in range(na):
            own_copy(ins, outs, sems, ai).start()

    def mid(ins, outs, sems):
        me, sibling, chips, c, copy = ctx(ins, outs, sems)
        for ai in range(na):
            for j, chip in enumerate(chips):
                copy(ai, 1 + j, (*chip, c), me).wait_recv()
                copy(ai, 4 + j, (*chip, c), sibling).start()

    def end(ins, outs, sems):
        me, sibling, chips, c, copy = ctx(ins, outs, sems)
        for ai in range(na):
            copy(ai, 0, sibling, me).wait_recv()
            copy(ai, 0, me, sibling, own=True).wait_send()
            for j, chip in enumerate(chips):
                copy(ai, 4 + j, (*chip, 1 - c), me).wait_recv()
                copy(ai, 1 + j, me, (*chip, c), own=True).wait_send()
                copy(ai, 4 + j, (*chip, c), sibling).wait_send()
            own_copy(ins, outs, sems, ai).wait()

    return Comm(arrays, [jax.ShapeDtypeStruct((N_DEV,) + a.shape, a.dtype) for a in arrays],
                [pltpu.SemaphoreType.DMA((na, 7)), pltpu.SemaphoreType.DMA((na, 7)), pltpu.SemaphoreType.DMA((na,))],
                start, end, mid)


def _swap_comm(arrays):
    na = len(arrays)
    offs = np.concatenate([[0], np.cumsum([a.shape[1] for a in arrays])]).astype(int)

    def copies(ins, outs, sems):
        x, y, c = _mesh_pos()
        return [pltpu.make_async_remote_copy(
            src_ref=ins[ai].at[2 * k + 1 - c], dst_ref=outs[0].at[k, pl.ds(int(offs[ai]), arrays[ai].shape[1])],
            send_sem=sems[0].at[ai, k], recv_sem=sems[1].at[ai, k], device_id=(x, y, 1 - c), device_id_type=MESH)
            for ai in range(na) for k in range(4)]

    def start(ins, outs, sems):
        for cp in copies(ins, outs, sems):
            cp.start()

    def end(ins, outs, sems):
        for cp in copies(ins, outs, sems):
            cp.wait()

    return Comm(arrays, [jax.ShapeDtypeStruct((4, int(offs[-1]), PACK_COLS), arrays[0].dtype)],
                [pltpu.SemaphoreType.DMA((na, 4)), pltpu.SemaphoreType.DMA((na, 4))], start, end)


def _chips_comm(send):
    def copies(ins, outs, sems):
        x, y, c = _mesh_pos()
        chips = [(1 - x, y), (x, 1 - y), (1 - x, 1 - y)]
        return [pltpu.make_async_remote_copy(
            src_ref=ins[0].at[2 * cx + cy], dst_ref=outs[0].at[j], send_sem=sems[0].at[j], recv_sem=sems[1].at[j],
            device_id=(cx, cy, c), device_id_type=MESH) for j, (cx, cy) in enumerate(chips)]

    def start(ins, outs, sems):
        for cp in copies(ins, outs, sems):
            cp.start()

    def end(ins, outs, sems):
        for cp in copies(ins, outs, sems):
            cp.wait()

    return Comm([send], [jax.ShapeDtypeStruct((3,) + send.shape[1:], send.dtype)],
                [pltpu.SemaphoreType.DMA((3,)), pltpu.SemaphoreType.DMA((3,))], start, end)


def _all_gather(arrays, name):
    na = len(arrays)

    def body(*refs):
        ins, outs = refs[:na], refs[na:2 * na]
        send_sems, recv_sems, local_sems = refs[2 * na:]
        x, y, c = _mesh_pos()
        me, sibling = (x, y, c), (x, y, 1 - c)
        chips = [(1 - x, y), (x, 1 - y), (1 - x, 1 - y)]
        waits = []
        for ai in range(na):
            in_ref, out_ref = ins[ai], outs[ai]

            def slot(px, py, pc, out_ref=out_ref):
                return out_ref.at[4 * px + 2 * py + pc]

            def copy(kk, block, to, src=None, ai=ai, slot=slot):
                return pltpu.make_async_remote_copy(
                    src_ref=slot(*block) if src is None else src, dst_ref=slot(*block),
                    send_sem=send_sems.at[ai, kk], recv_sem=recv_sems.at[ai, kk], device_id=to, device_id_type=MESH)

            mine = pltpu.make_async_copy(in_ref, slot(*me), local_sems.at[ai])
            mine.start()
            first = [copy(0, me, sibling, src=in_ref)]
            first += [copy(1 + j, me, (*chip, c), src=in_ref) for j, chip in enumerate(chips)]
            for cp in first:
                cp.start()
            waits.append((copy, mine, first))
        sends = []
        for ai in range(na):
            copy, mine, first = waits[ai]
            passed = [copy(4 + j, (*chip, c), sibling) for j, chip in enumerate(chips)]
            for j, chip in enumerate(chips):
                copy(1 + j, (*chip, c), me).wait_recv()
                passed[j].start()
            sends.append(passed)
        for ai in range(na):
            copy, mine, first = waits[ai]
            copy(0, sibling, me).wait_recv()
            for j, chip in enumerate(chips):
                copy(4 + j, (*chip, 1 - c), me).wait_recv()
            for cp in first + sends[ai]:
                cp.wait_send()
            mine.wait()

    any_spec = pl.BlockSpec(memory_space=pl.ANY)
    return pl.pallas_call(
        body, name=name, in_specs=[any_spec] * na, out_specs=[any_spec] * na,
        out_shape=[jax.ShapeDtypeStruct((N_DEV,) + a.shape, a.dtype) for a in arrays],
        scratch_shapes=[pltpu.SemaphoreType.DMA((na, 7)), pltpu.SemaphoreType.DMA((na, 7)),
                        pltpu.SemaphoreType.DMA((na,))],
        compiler_params=pltpu.CompilerParams(has_side_effects=True))(*arrays)


def _swap_sibling(arrays, name):
    na = len(arrays)
    offs = np.concatenate([[0], np.cumsum([a.shape[1] for a in arrays])]).astype(int)
    rows = int(offs[-1])

    def body(*refs):
        ins, recv_ref = refs[:na], refs[na]
        send_sems, recv_sems = refs[na + 1:]
        x, y, c = _mesh_pos()
        started = []
        for ai in range(na):
            span = pl.ds(int(offs[ai]), arrays[ai].shape[1])
            for k in range(4):
                remote = pltpu.make_async_remote_copy(
                    src_ref=ins[ai].at[2 * k + 1 - c], dst_ref=recv_ref.at[k, span], send_sem=send_sems.at[ai, k],
                    recv_sem=recv_sems.at[ai, k], device_id=(x, y, 1 - c), device_id_type=MESH)
                remote.start()
                started.append(remote)
        for remote in started:
            remote.wait()

    any_spec = pl.BlockSpec(memory_space=pl.ANY)
    return pl.pallas_call(
        body, name=name, in_specs=[any_spec] * na, out_specs=any_spec,
        out_shape=jax.ShapeDtypeStruct((4, rows, PACK_COLS), arrays[0].dtype),
        scratch_shapes=[pltpu.SemaphoreType.DMA((na, 4)), pltpu.SemaphoreType.DMA((na, 4))])(*arrays)


def _exchange_chips(send, name):
    def body(s_ref, o_ref, send_sems, recv_sems):
        x, y, c = _mesh_pos()
        chips = [(1 - x, y), (x, 1 - y), (1 - x, 1 - y)]
        cps = [pltpu.make_async_remote_copy(
            src_ref=s_ref.at[2 * cx + cy], dst_ref=o_ref.at[j], send_sem=send_sems.at[j], recv_sem=recv_sems.at[j],
            device_id=(cx, cy, c), device_id_type=MESH) for j, (cx, cy) in enumerate(chips)]
        for cp in cps:
            cp.start()
        for cp in cps:
            cp.wait()

    any_spec = pl.BlockSpec(memory_space=pl.ANY)
    return pl.pallas_call(
        body, name=name, in_specs=[any_spec], out_specs=any_spec,
        out_shape=jax.ShapeDtypeStruct((3,) + send.shape[1:], send.dtype),
        scratch_shapes=[pltpu.SemaphoreType.DMA((3,)), pltpu.SemaphoreType.DMA((3,))])(send)


def _pair_sum(keep, recv, name, tr=464):
    nchip, rows, cols = keep.shape

    def body(g_ref, r_ref, o_ref):
        o_ref[...] = (g_ref[...].astype(F32) + r_ref[...].astype(F32)).astype(BF16)

    blk = pl.BlockSpec((1, tr, cols), lambda k, i: (k, i, 0))
    return pl.pallas_call(
        body, name=name, grid=(nchip, rows // tr), in_specs=[blk, blk], out_specs=blk,
        out_shape=jax.ShapeDtypeStruct((nchip, rows, cols), BF16),
        compiler_params=_params(("parallel", "parallel")))(keep, recv)


def _chip_sum(own, others, name, tr=464):
    _, rows, cols = own.shape
    chip = (2 * lax.axis_index("x") + lax.axis_index("y")).astype(jnp.int32).reshape(1)

    def body(chip_ref, own_ref, oth_ref, o_ref):
        del chip_ref
        acc = own_ref[0].astype(F32)
        for j in range(3):
            acc = acc + oth_ref[j].astype(F32)
        o_ref[...] = acc

    grid_spec = pltpu.PrefetchScalarGridSpec(
        num_scalar_prefetch=1, grid=(rows // tr,),
        in_specs=[pl.BlockSpec((1, tr, cols), lambda i, chip_ref: (chip_ref[0], i, 0)),
                  pl.BlockSpec((3, tr, cols), lambda i, chip_ref: (0, i, 0))],
        out_specs=pl.BlockSpec((tr, cols), lambda i, chip_ref: (i, 0)))
    return pl.pallas_call(
        body, name=name, grid_spec=grid_spec, out_shape=jax.ShapeDtypeStruct((rows, cols), F32),
        compiler_params=_params(("parallel",)))(chip, own, others)


def _sum_leading(parts, name, tr=464):
    nparts, rows, cols = parts.shape
    tr = tr if rows % tr == 0 else rows

    def body(p_ref, o_ref):
        acc = p_ref[0].astype(F32)
        for i in range(1, nparts):
            acc = acc + p_ref[i].astype(F32)
        o_ref[...] = acc

    return pl.pallas_call(
        body, name=name, grid=(rows // tr,),
        in_specs=[pl.BlockSpec((nparts, tr, cols), lambda i: (0, i, 0))],
        out_specs=pl.BlockSpec((tr, cols), lambda i: (i, 0)), out_shape=jax.ShapeDtypeStruct((rows, cols), F32),
        compiler_params=_params(("parallel",)))(parts)


def _adamw(w, g, m, v, name):
    shape = w.shape
    cols = shape[-1]
    rows = int(np.prod(shape[:-1])) if len(shape) > 1 else 1
    w2, g2, m2, v2 = (a.reshape(rows, cols) for a in (w, g, m, v))
    tr = rows
    for cand in (512, 256, 128, 64, 32, 16, 8):
        if rows % cand == 0 and rows > cand:
            tr = cand
            break
    bc1, bc2 = 1.0 - ADAM_B1 ** ADAM_STEP, 1.0 - ADAM_B2 ** ADAM_STEP

    def body(w_ref, g_ref, m_ref, v_ref, d_ref, nm_ref, nv_ref):
        gv = g_ref[...]
        nm = ADAM_B1 * m_ref[...] + (1.0 - ADAM_B1) * gv
        nv = ADAM_B2 * v_ref[...] + (1.0 - ADAM_B2) * (gv * gv)
        nm_ref[...] = nm
        nv_ref[...] = nv
        d_ref[...] = -ADAM_LR * ((nm / bc1) / (jnp.sqrt(nv / bc2) + ADAM_EPS) + ADAM_WD * w_ref[...])

    blk = pl.BlockSpec((tr, cols), lambda i: (i, 0))
    outs = pl.pallas_call(
        body, name=name, grid=(rows // tr,), in_specs=[blk] * 4, out_specs=[blk] * 3,
        out_shape=[jax.ShapeDtypeStruct((rows, cols), F32)] * 3, compiler_params=_params(("parallel",)))(w2, g2, m2, v2)
    return tuple(o.reshape(shape) for o in outs)


WEIGHT_NAMES = ['norm_mix_g', 'norm_xa_g', 'norm_ffn_g', 'norm_mem_g', 'norm_final_g', 'w_in_ab', 'conv_qkv_a',
                'a_log_a', 'dt_bias_a', 'onorm_g_a', 'ssm_lambda_re', 'ssm_lambda_im', 'ssm_b_re', 'ssm_b_im',
                'ssm_c_re', 'ssm_c_im', 'ssm_d', 'ssm_log_dt', 'w_glu_b', 'b_glu_b', 'w_out_ab', 'pool_w',
                'pool_scale', 'xa_wq', 'xa_wkv', 'xa_wo', 'ffn_w_up', 'ffn_conv', 'ffn_w_down']
BIG_SHARDED = {'w_in_ab': ((1, 1024, 2568), 2), 'w_glu_b': ((1, 512, 512), 1), 'w_out_ab': ((1, 1024, 1024), 1),
               'pool_w': ((1, 4, 256, 256), 2), 'xa_wq': ((2, 1024, 1024), 1), 'xa_wkv': ((2, 1024, 2048), 2),
               'xa_wo': ((2, 1024, 1024), 1), 'ffn_w_up': ((2, 1024, 5632), 2), 'ffn_w_down': ((2, 2816, 1024), 1)}
SMALL_SHARDED = {'conv_qkv_a': ((1, 4, 1536), 2), 'pool_scale': ((1, 1024), 1), 'ffn_conv': ((2, 3, 5632), 2)}
REPLICATED = {'norm_mix_g': (2, 1024), 'norm_xa_g': (2, 1024), 'norm_ffn_g': (2, 1024), 'norm_mem_g': (1024,),
              'norm_final_g': (1024,), 'a_log_a': (1, 4), 'dt_bias_a': (1, 4), 'onorm_g_a': (1, 128),
              'ssm_lambda_re': (1, 32, 64), 'ssm_lambda_im': (1, 32, 64), 'ssm_b_re': (1, 32, 64, 16),
              'ssm_b_im': (1, 32, 64, 16), 'ssm_c_re': (1, 32, 16, 64), 'ssm_c_im': (1, 32, 16, 64),
              'ssm_d': (1, 32, 16), 'ssm_log_dt': (1, 32), 'b_glu_b': (1, 512)}
PACK_ROW_ALIGN = 8


def _shard_shape(shape, axis):
    return tuple(s // N_DEV if i == axis else s for i, s in enumerate(shape))


def _round_up(n, m):
    return (n + m - 1) // m * m


def _pack(arrays):
    total = sum(int(np.prod(a.shape)) for a in arrays)
    padded = _round_up(total, PACK_COLS * PACK_ROW_ALIGN)
    parts = [a.astype(F32).reshape(-1) for a in arrays]
    if padded != total:
        parts.append(jnp.zeros((padded - total,), F32))
    return jnp.concatenate(parts).reshape(padded // PACK_COLS, PACK_COLS)


def _unpack(packed, shapes):
    flat, out, off = packed.reshape(-1), [], 0
    for shape in shapes:
        size = int(np.prod(shape))
        out.append(flat[off:off + size].reshape(shape))
        off += size
    return out


def _split_shards(full, axis):
    shape = full.shape
    s = shape[axis] // N_DEV
    a = full.reshape(shape[:axis] + (N_DEV, s) + shape[axis + 1:])
    return jnp.moveaxis(a, axis, 0).reshape(N_DEV, -1)


def _merge_shards(pieces, shape, axis):
    sh = _shard_shape(shape, axis)
    a = pieces.reshape((N_DEV,) + sh)
    a = jnp.moveaxis(a, 0, axis)
    return a.reshape(shape)


_SCAN_NB = SSM_CH // SCAN_CB


def _to_scan_layout(m, axis):
    shape = m.shape
    m = m.reshape(shape[:axis] + (2, _SCAN_NB, SCAN_CB) + shape[axis + 1:])
    return jnp.swapaxes(m, axis, axis + 1).reshape(shape)


def _from_scan_layout(m, axis):
    shape = m.shape
    m = m.reshape(shape[:axis] + (_SCAN_NB, 2, SCAN_CB) + shape[axis + 1:])
    return jnp.swapaxes(m, axis, axis + 1).reshape(shape)


def _s5_discretise(lam_re, lam_im, b_re, b_im, log_dt):
    dt = jnp.exp(log_dt)[:, None]
    mag = jnp.exp(lam_re * dt)
    ang = lam_im * dt
    lb_re, lb_im = mag * jnp.cos(ang), mag * jnp.sin(ang)
    den = lam_re * lam_re + lam_im * lam_im
    nr, ni = lb_re - 1.0, lb_im
    coef_re = (nr * lam_re + ni * lam_im) / den
    coef_im = (ni * lam_re - nr * lam_im) / den
    bb_re = coef_re[..., None] * b_re - coef_im[..., None] * b_im
    bb_im = coef_re[..., None] * b_im + coef_im[..., None] * b_re
    return lb_re, lb_im, bb_re, bb_im


_GROUPS_PER_BLOCK = N_GROUPS // _SCAN_NB
_U_BLOCK = _GROUPS_PER_BLOCK * SSM_GROUP


def _s5_matrices(lb_re, lb_im, bb_re, bb_im, c_re, c_im):
    eye = jnp.eye(_GROUPS_PER_BLOCK, dtype=F32)
    blocked = lambda m: m.reshape((_SCAN_NB, _GROUPS_PER_BLOCK) + m.shape[1:])
    bmat = lambda bb: jnp.einsum('jgph,gk->jghkp', blocked(bb), eye).reshape(_SCAN_NB, _U_BLOCK, SCAN_CB)
    cmat = lambda cc: jnp.einsum('jghp,gk->jkpgh', blocked(cc), eye).reshape(_SCAN_NB, SCAN_CB, _U_BLOCK)
    b_in = jnp.concatenate([bmat(bb_re), bmat(bb_im)], axis=2)
    c_out = jnp.concatenate([cmat(c_re), -cmat(c_im)], axis=1)
    a_row = _to_scan_layout(jnp.concatenate([lb_re.reshape(1, SSM_CH), lb_im.reshape(1, SSM_CH)], axis=1), 1)
    return b_in, c_out, a_row


def _s5_matrix_grads(db_in, dc_out, da_row):
    da_nat = _from_scan_layout(da_row, 1)
    eye = jnp.eye(_GROUPS_PER_BLOCK, dtype=F32)
    nb, gb = _SCAN_NB, _GROUPS_PER_BLOCK
    bgrad = lambda m: jnp.einsum('jghkp,gk->jgph', m.reshape(nb, gb, SSM_GROUP, gb, SSM_STATE), eye
                                 ).reshape(N_GROUPS, SSM_STATE, SSM_GROUP)
    cgrad = lambda m: jnp.einsum('jkpgh,gk->jghp', m.reshape(nb, gb, SSM_STATE, gb, SSM_GROUP), eye
                                 ).reshape(N_GROUPS, SSM_GROUP, SSM_STATE)
    dbb_re, dbb_im = bgrad(db_in[:, :, :SCAN_CB]), bgrad(db_in[:, :, SCAN_CB:])
    dc_re, dc_im = cgrad(dc_out[:, :SCAN_CB]), -cgrad(dc_out[:, SCAN_CB:])
    dlb_re = da_nat[0, :SSM_CH].reshape(N_GROUPS, SSM_STATE)
    dlb_im = da_nat[0, SSM_CH:].reshape(N_GROUPS, SSM_STATE)
    return dlb_re, dlb_im, dbb_re, dbb_im, dc_re, dc_im


def _as_pieces(a):
    return a.reshape(N_DEV, a.shape[0] // N_DEV, a.shape[1])


def _hybrid_fwd(xn, x, wts, p, weights, riders):
    sv = {}
    hq = _mm(xn, wts['w_qkv_t'], "nt", "l0_in_qkv")
    gate = _mm(xn, wts['w_gate_t'], "nt", "l0_in_gate")
    ba = _mm(xn, wts['w_ba_t'], "nt", "l0_in_ba")
    u = _mm(xn, wts['w_u_t'], "nt", "l0_in_u")
    conv = p['conv_qkv']
    q = _qkv_pre_fwd(hq, conv, 0, 4, True, HEAD_A ** -0.5, "l0_q_pre")
    k = _qkv_pre_fwd(hq, conv, 4, 4, True, 1.0, "l0_k_pre")
    v = _qkv_pre_fwd(hq, conv, 8, 4, False, 1.0, "l0_v_pre")
    gates = _gates_fwd(ba, p['arow'], p['brow'], "l0_gates")
    o, tm_all, s_all = riders.run("l0_gdr_fwd", _gdr_fwd, q, k, v, gates)
    wts['w_glu'], wts['w_out'] = weights.full['w_glu'], weights.full['w_out']
    y_a = _onorm_fwd(o, gate, p['onorm_g'], "l0_onorm")
    bu = riders.run("l0_s5_bu", _mm_bd, u, p['b_in'], "nn")
    xs = riders.run("l0_s5_scan", _s5_scan_fwd, bu, p['a_row'])
    yc = riders.run("l0_s5_cx", _mm_bd, xs, p['c_out'], "nn")
    yl, y_b = _glu_fwd(yc, u, p['d_row'], wts['w_glu'], p['b_glu'], "l0_glu")
    mixed = jnp.concatenate([y_a, y_b], axis=1)
    x1 = _mm(mixed, wts['w_out'], "nn", "l0_out", res=x)
    sv.update(hq=hq, gate=gate, ba=ba, u=u, q=q, k=k, v=v, gb=gates, o=o, tm=tm_all, s=s_all, xs=xs, yl=yl, mixed=mixed)
    return x1, sv


def _hybrid_bwd(dx1, xn, wts, p, sv, riders):
    gr = {}
    dmixed = _mm(dx1, wts['w_out'], "nt", "l0_out_dx", out_dtype=BF16)
    riders.grad('w_out', _as_pieces(_mm(sv['mixed'], dx1, "tn", "l0_out_dw", out_dtype=BF16)))
    dya, dyb = dmixed[:, :WIDTH_A], dmixed[:, WIDTH_A:]
    dyl, du_direct, dw_glu, gr['b_glu_b'], dd = _glu_bwd(
        sv['yl'], sv['u'], p['d_row'], wts['w_glu'], p['b_glu'], dyb, "l0_glu_bwd")
    riders.grad('w_glu', dw_glu.astype(BF16).reshape(N_DEV, -1, PACK_COLS))
    dxs = riders.run("l0_s5_cx_dx", _mm_bd, dyl, p['c_out'], "nt")
    dc_out = _mm_bd(sv['xs'], dyl, "tn", "l0_s5_cx_dw")
    lam, da_row = riders.run("l0_s5_scan_bwd", _s5_scan_bwd, dxs, sv['xs'], p['a_row'])
    du = _mm_bd(lam, p['b_in'], "nt", "l0_s5_bu_dx", res=du_direct, out_dtype=BF16)
    db_in = _mm_bd(sv['u'], lam, "tn", "l0_s5_bu_dw")
    gr['s5'] = (db_in, dc_out, da_row, dd)
    do, dgate, gr['onorm_g_a'] = _onorm_bwd(sv['o'], sv['gate'], p['onorm_g'], dya, "l0_onorm_bwd")
    dq, dk, dv, dgb = riders.run("l0_gdr_bwd", _gdr_bwd, sv['q'], sv['k'], sv['v'], sv['gb'], sv['tm'], sv['s'], do)
    conv = p['conv_qkv']
    dhq_q, dcw_q = _qkv_pre_bwd(sv['hq'], conv, dq, 0, 4, True, HEAD_A ** -0.5, "l0_q_pre_bwd")
    dhq_k, dcw_k = _qkv_pre_bwd(sv['hq'], conv, dk, 4, 4, True, 1.0, "l0_k_pre_bwd")
    dhq_v, dcw_v = _qkv_pre_bwd(sv['hq'], conv, dv, 8, 4, False, 1.0, "l0_v_pre_bwd")
    gr['conv_qkv_a'] = jnp.concatenate([dcw_q, dcw_k, dcw_v], axis=1)
    dhq = jnp.concatenate([dhq_q, dhq_k, dhq_v], axis=1)
    dba, da_log, ddt_bias = _gates_bwd(sv['ba'], p['arow'], p['brow'], dgb, "l0_gates_bwd")
    gr['a_log_a'], gr['dt_bias_a'] = da_log[:, 4:8], ddt_bias[:, 4:8]
    dxn = _mm(dhq, wts['w_qkv_t'], "nn", "l0_in_qkv_dx")
    dxn = _mm(dgate, wts['w_gate_t'], "nn", "l0_in_gate_dx", res=dxn)
    dxn = _mm(dba, wts['w_ba_t'], "nn", "l0_in_ba_dx", res=dxn)
    dxn = _mm(du, wts['w_u_t'], "nn", "l0_in_u_dx", res=dxn)
    dw_qkv_t = _mm(dhq, xn, "tn", "l0_in_qkv_dw", out_dtype=BF16)
    dw_gate_t = _mm(dgate, xn, "tn", "l0_in_gate_dw", out_dtype=BF16)
    dw_ba_t = _mm(dba, xn, "tn", "l0_in_ba_dw", out_dtype=BF16)
    dw_u_t = _mm(du, xn, "tn", "l0_in_u_dw", out_dtype=BF16)
    dw_in_t = _as_pieces(jnp.concatenate([dw_qkv_t, dw_gate_t, dw_ba_t[:8], dw_u_t], axis=0))
    riders.grad('w_in_t', jnp.concatenate(
        [dw_in_t, jnp.zeros((N_DEV, dict(PIECES)['w_in_t'] - W_IN_PIECE, D_MODEL), BF16)], axis=1))
    return dxn, gr


def _xa_fwd(x1, g, mem_n, wq, wkv_t, wo, tag, riders):
    xq = _rms_fwd(x1, g, BF16, tag + "_norm")
    q = _mm(xq, wq, "nn", tag + "_q", out_dtype=BF16)
    kv = _mm(mem_n, wkv_t, "nt", tag + "_kv", out_dtype=BF16)
    o = riders.run(tag + "_attn", _attn_fwd, q, kv)
    x2 = _mm(o, wo, "nn", tag + "_o", res=x1)
    return x2, dict(xq=xq, q=q, kv=kv, o=o)


def _xa_bwd(dx2, x1, g, mem_n, wq, wkv_t, wo, sv, tag, layer, riders):
    do = _mm(dx2, wo, "nt", tag + "_o_dx", out_dtype=BF16)
    riders.grad('wo%d' % layer, _as_pieces(_mm(sv['o'], dx2, "tn", tag + "_o_dw", out_dtype=BF16)))
    dq, dk, dv = _attn_bwd(sv['q'], sv['kv'], do, tag + "_attn_bwd")
    dkv = jnp.concatenate([dk, dv], axis=1).astype(BF16)
    dxq = _mm(dq, wq, "nt", tag + "_q_dx")
    riders.grad('wq%d' % layer, _as_pieces(_mm(sv['xq'], dq, "tn", tag + "_q_dw", out_dtype=BF16)))
    dmem_n = _mm(dkv, wkv_t, "nn", tag + "_kv_dx")
    riders.grad('wkv_t%d' % layer, _as_pieces(_mm(dkv, mem_n, "tn", tag + "_kv_dw", out_dtype=BF16)))
    dx1, dg = riders.run(tag + "_norm_bwd", _rms_bwd, x1, g, dxq, dx2)
    return dx1, dmem_n, dg


def _ffn_fwd(x2, g, w_up_t, conv, w_down, tag, riders):
    xf = _rms_fwd(x2, g, BF16, tag + "_norm")
    h = riders.run(tag + "_up", _mm, xf, w_up_t, "nt")
    a = riders.run(tag + "_act", _ffn_act_fwd, h, conv)
    x3 = _mm(a, w_down, "nn", tag + "_down", res=x2)
    return x3, dict(xf=xf, h=h, a=a)


def _ffn_bwd(dx3, x2, g, w_up_t, conv, w_down, sv, tag, layer, riders):
    da = _mm(dx3, w_down, "nt", tag + "_down_dx")
    riders.grad('down%d' % layer, _as_pieces(_mm(sv['a'], dx3, "tn", tag + "_down_dw", out_dtype=BF16)))
    dh, dconv = riders.run(tag + "_act_bwd", _ffn_act_bwd, sv['h'], conv, da)
    dxf = riders.run(tag + "_up_dx", _mm, dh, w_up_t, "nn")
    dw_up_t = riders.run(tag + "_up_dw", _mm, dh, sv['xf'], "tn", out_dtype=BF16)
    riders.grad('up_t%d' % layer, _as_pieces(dw_up_t))
    dx2, dg = riders.run(tag + "_norm_bwd", _rms_bwd, x2, g, dxf, dx3)
    return dx2, dconv, dg


BIG_NAMES, SMALL_NAMES, REP_NAMES = list(BIG_SHARDED), list(SMALL_SHARDED), list(REPLICATED)
BIG_SIZES = [int(np.prod(_shard_shape(*BIG_SHARDED[n]))) for n in BIG_NAMES]
SMALL_SIZES = [int(np.prod(_shard_shape(*SMALL_SHARDED[n]))) for n in SMALL_NAMES]


PIECES = [('w_in_t', 384), ('w_glu', 32), ('w_out', 128), ('pool_w', 32), ('wq0', 128), ('wq1', 128),
          ('wkv_t0', 256), ('wkv_t1', 256), ('wo0', 128), ('wo1', 128), ('up_t0', 704), ('up_t1', 704),
          ('down0', 352), ('down1', 352)]
PIECE_OFFS = dict(zip([k for k, _ in PIECES], np.concatenate([[0], np.cumsum([r for _, r in PIECES])[:-1]]).tolist()))
W_IN_ROWS = 4 * WIDTH_A + 2 * N_HEADS_A + SSM_WIDTH
W_IN_PIECE = W_IN_ROWS // N_DEV


def _row_tile(rows):
    return max(t for t in range(16, min(rows, 512) + 1, 16) if rows % t == 0)


class _Riders:
    def __init__(self):
        self.waiting = {}
        self.grads = {}
        self.groups = []
        self.reduced = {}

    def add(self, host, comm, then):
        self.waiting.setdefault(host, []).append((comm, then))

    def run(self, name, fn, *args, **kw):
        riders = self.waiting.pop(name, [])
        if not riders:
            return fn(*args, name=name, **kw)
        out, couts = fn(*args, name=name, comm=[c for c, _ in riders], **kw)
        for (_, then), got in zip(riders, couts):
            then(got)
        return out

    def exchange(self, comm, host, name, then):
        if host is None:
            then(_comm_only(comm, name))
        else:
            self.add(host, comm, then)

    def grad(self, key, pieces):
        self.grads[key] = pieces
        for group in [g for g in self.groups if all(k in self.grads for k in g[1])]:
            self.groups.remove(group)
            self._reduce(*group)

    def _reduce(self, name, keys, swap_host, chips_host):
        arrays = [self.grads[k] for k in keys]
        rows = sum(a.shape[1] for a in arrays)
        tile = _row_tile(rows)

        def after_chips(chip_sums, got):
            total = _chip_sum(chip_sums, got[0], name + "_chip_sum", tr=tile)
            off = 0
            for k, a in zip(keys, arrays):
                self.reduced[k] = total[off:off + a.shape[1]]
                off += a.shape[1]

        def after_swap(got):
            core = lax.axis_index("c")
            keep = jnp.concatenate(
                [lax.dynamic_index_in_dim(a.reshape(4, 2, a.shape[1], PACK_COLS), core, 1, keepdims=False)
                 for a in arrays], axis=1)
            chip_sums = _pair_sum(keep, got[0], name + "_pair_sum", tr=tile)
            self.exchange(_chips_comm(chip_sums), chips_host, name + "_to_chips",
                          functools.partial(after_chips, chip_sums))

        self.exchange(_swap_comm(arrays), swap_host, name + "_to_sibling", after_swap)


class _Weights:
    def __init__(self, inp):
        bf = lambda a: a.astype(BF16)
        local = {'w_in_t': bf(inp['w_in_ab'][0]).T, 'w_glu': bf(inp['w_glu_b'][0]), 'w_out': bf(inp['w_out_ab'][0]),
                 'pool_w': bf(inp['pool_w'][0]),
                 'small': _pack([inp[n] for n in SMALL_NAMES])}
        for l in range(2):
            local['wq%d' % l] = bf(inp['xa_wq'][l])
            local['wkv_t%d' % l] = bf(inp['xa_wkv'][l]).T
            local['wo%d' % l] = bf(inp['xa_wo'][l])
            local['up_t%d' % l] = bf(inp['ffn_w_up'][l]).T
            local['down%d' % l] = bf(inp['ffn_w_down'][l])
        self.local, self.full = local, {}

    def plan(self, keys):
        return _gather_comm([self.local[k] for k in keys])

    def land(self, keys, gathered):
        for k, g in zip(keys, gathered):
            if k == 'small':
                off = 0
                for n, size in zip(SMALL_NAMES, SMALL_SIZES):
                    self.full[n] = _merge_shards(g.reshape(N_DEV, -1)[:, off:off + size], *SMALL_SHARDED[n])
                    off += size
            elif k == 'pool_w':
                self.full[k] = jnp.swapaxes(g, 0, 1).reshape(len(POOL_WINDOWS), POOL_GROUP, POOL_GROUP)
            else:
                self.full[k] = g.reshape(N_DEV * g.shape[1], g.shape[2])


GATHER_FIRST = ['w_in_t', 'small']
GATHER_RIDES = [('l0_gdr_fwd', ['w_glu', 'w_out', 'wq0', 'wkv_t0', 'wo0', 'up_t0']),
                ('l0_s5_bu', ['pool_w', 'wq1']), ('l0_s5_scan', ['down0']), ('l0_s5_cx', ['wo1']),
                ('l0_xa_attn', ['wkv_t1']), ('l0_ffn_up', ['up_t1']), ('l0_ffn_act', ['down1'])]
GRAD_RIDES = [('g_down1', ['down1'], 'l1_ffn_act_bwd', 'l1_ffn_up_dx'),
              ('g_up1', ['up_t1'], 'l1_ffn_norm_bwd', 'l0_ffn_act_bwd'),
              ('g_xa1', ['wq1', 'wkv_t1', 'wo1', 'pool_w'], 'l1_mix_norm_bwd', 'l0_ffn_up_dx'),
              ('g_down0', ['down0'], 'l0_ffn_act_bwd', 'l0_ffn_up_dw'),
              ('g_up0', ['up_t0'], 'l0_ffn_norm_bwd', 'l0_gdr_bwd'),
              ('g_xa0', ['wq0', 'wkv_t0', 'wo0'], 'l0_xa_norm_bwd', 'l0_gdr_bwd'),
              ('g_out', ['w_out', 'w_glu'], 'l0_s5_cx_dx', 'l0_s5_scan_bwd'),
              ('g_in', ['w_in_t'], None, None)]


def _local_step(inp):
    f32_of = lambda n: inp[n].astype(F32)
    weights = _Weights(inp)
    riders = _Riders()
    riders.groups = list(GRAD_RIDES)
    full = weights.full
    weights.land(GATHER_FIRST, _comm_only(weights.plan(GATHER_FIRST), "gather_first"))
    for host, keys in GATHER_RIDES:
        riders.add(host, weights.plan(keys), functools.partial(weights.land, keys))
    w_in_t = full['w_in_t']
    wts0 = dict(w_qkv_t=w_in_t[:3 * WIDTH_A], w_gate_t=w_in_t[3 * WIDTH_A:4 * WIDTH_A],
                w_ba_t=jnp.concatenate([w_in_t[4 * WIDTH_A:4 * WIDTH_A + 8], jnp.zeros((LANE - 8, D_MODEL), BF16)], 0),
                w_u_t=w_in_t[4 * WIDTH_A + 8:])
    lb_disc, disc_vjp = jax.vjp(_s5_discretise, f32_of('ssm_lambda_re')[0], f32_of('ssm_lambda_im')[0],
                                f32_of('ssm_b_re')[0], f32_of('ssm_b_im')[0], f32_of('ssm_log_dt')[0])
    b_in, c_out, a_row = _s5_matrices(*lb_disc, f32_of('ssm_c_re')[0], f32_of('ssm_c_im')[0])
    zeros4 = jnp.zeros((1, 4), F32)
    p0 = dict(conv_qkv=full['conv_qkv_a'][0], onorm_g=f32_of('onorm_g_a'),
              arow=jnp.concatenate([zeros4, f32_of('a_log_a'), jnp.zeros((1, LANE - 8), F32)], 1),
              brow=jnp.concatenate([zeros4, f32_of('dt_bias_a'), jnp.zeros((1, LANE - 8), F32)], 1),
              b_in=b_in.astype(BF16), c_out=c_out.astype(BF16), a_row=a_row,
              d_row=f32_of('ssm_d').reshape(1, SSM_WIDTH), b_glu=f32_of('b_glu_b'))

    x0 = inp['x'][0]
    mem_n = _rms_fwd(inp['mem'][0], inp['norm_mem_g'], BF16, "mem_norm")
    xn0 = _rms_fwd(x0, inp['norm_mix_g'][0], BF16, "l0_mix_norm")
    x1, sv_mix0 = _hybrid_fwd(xn0, x0, wts0, p0, weights, riders)
    x2, sv_xa0 = _xa_fwd(x1, inp['norm_xa_g'][0], mem_n, full['wq0'], full['wkv_t0'], full['wo0'], "l0_xa", riders)
    x3, sv_ffn0 = _ffn_fwd(x2, inp['norm_ffn_g'][0], full['up_t0'], full['ffn_conv'][0], full['down0'], "l0_ffn", riders)
    xn1 = _rms_fwd(x3, inp['norm_mix_g'][1], F32, "l1_mix_norm")
    x4 = _pool_fwd(xn1, full['pool_w'], full['pool_scale'], x3, "l1_pool")
    x5, sv_xa1 = _xa_fwd(x4, inp['norm_xa_g'][1], mem_n, full['wq1'], full['wkv_t1'], full['wo1'], "l1_xa", riders)
    x6, sv_ffn1 = _ffn_fwd(x5, inp['norm_ffn_g'][1], full['up_t1'], full['ffn_conv'][1], full['down1'], "l1_ffn", riders)
    loss_part, dx6, dg_final = _loss_head(x6, inp['norm_final_g'], inp['loss_target'][0], "loss_head")

    dx5, dconv1, dg_ffn1 = _ffn_bwd(dx6, x5, inp['norm_ffn_g'][1], full['up_t1'], full['ffn_conv'][1], full['down1'],
                                    sv_ffn1, "l1_ffn", 1, riders)
    dx4, dmem1, dg_xa1 = _xa_bwd(dx5, x4, inp['norm_xa_g'][1], mem_n, full['wq1'], full['wkv_t1'], full['wo1'],
                                 sv_xa1, "l1_xa", 1, riders)
    dxn1, dpool_w, dpool_scale = _pool_bwd(xn1, full['pool_w'], full['pool_scale'], dx4, "l1_pool_bwd")
    pool_pieces = jnp.swapaxes(dpool_w.astype(BF16).reshape(len(POOL_WINDOWS), N_DEV, -1, POOL_GROUP), 0, 1)
    riders.grad('pool_w', pool_pieces.reshape(N_DEV, -1, PACK_COLS))
    dx3, dg_mix1 = riders.run("l1_mix_norm_bwd", _rms_bwd, x3, inp['norm_mix_g'][1], dxn1, dx4)
    dx2, dconv0, dg_ffn0 = _ffn_bwd(dx3, x2, inp['norm_ffn_g'][0], full['up_t0'], full['ffn_conv'][0], full['down0'],
                                    sv_ffn0, "l0_ffn", 0, riders)
    dx1, dmem0, dg_xa0 = _xa_bwd(dx2, x1, inp['norm_xa_g'][0], mem_n, full['wq0'], full['wkv_t0'], full['wo0'],
                                 sv_xa0, "l0_xa", 0, riders)
    dxn0, g_mix0 = _hybrid_bwd(dx1, xn0, wts0, p0, sv_mix0, riders)
    grad_x, dg_mix0 = _rms_bwd(x0, inp['norm_mix_g'][0], dxn0, dx1, "l0_mix_norm_bwd")
    _, dg_mem = _rms_bwd(inp['mem'][0], inp['norm_mem_g'], dmem0 + dmem1, None, "mem_norm_bwd")
    assert not riders.waiting and not riders.groups, (list(riders.waiting), riders.groups)

    db_in, dc_out, da_row, dd = g_mix0['s5']
    dlb_re, dlb_im, dbb_re, dbb_im, dc_re, dc_im = _s5_matrix_grads(db_in, dc_out, da_row)
    dlam_re, dlam_im, dbr, dbi, dlog_dt = disc_vjp((dlb_re, dlb_im, dbb_re, dbb_im))

    rep_grads = {
        'norm_mix_g': jnp.concatenate([dg_mix0, dg_mix1], 0), 'norm_xa_g': jnp.concatenate([dg_xa0, dg_xa1], 0),
        'norm_ffn_g': jnp.concatenate([dg_ffn0, dg_ffn1], 0), 'norm_mem_g': dg_mem.reshape(-1),
        'norm_final_g': dg_final.reshape(-1), 'a_log_a': g_mix0['a_log_a'], 'dt_bias_a': g_mix0['dt_bias_a'],
        'onorm_g_a': g_mix0['onorm_g_a'], 'ssm_lambda_re': dlam_re[None], 'ssm_lambda_im': dlam_im[None],
        'ssm_b_re': dbr[None], 'ssm_b_im': dbi[None], 'ssm_c_re': dc_re[None], 'ssm_c_im': dc_im[None],
        'ssm_d': dd.reshape(1, N_GROUPS, SSM_GROUP), 'ssm_log_dt': dlog_dt[None], 'b_glu_b': g_mix0['b_glu_b']}
    small_grads = {'conv_qkv_a': g_mix0['conv_qkv_a'][None], 'pool_scale': dpool_scale,
                   'ffn_conv': jnp.stack([dconv0, dconv1])}
    for key, rows in PIECES:
        assert riders.reduced[key].shape == (rows, PACK_COLS), key
    return loss_part, grad_x, riders.reduced, rep_grads, small_grads


def _reduce_small_gradients(rep_grads, small_grads):
    misc_local = _pack([rep_grads[n] for n in REP_NAMES] + [small_grads[n] for n in SMALL_NAMES])
    (misc_all,) = _comm_only(_gather_comm([misc_local]), "gather_small_grads")
    return _sum_leading(misc_all, "small_grads_sum")


def _update(inp, loss_part, grad_x, big_reduced, misc_sum):
    dev = _device_index()
    piece = lambda key, rows=None: big_reduced[key] if rows is None else big_reduced[key][:rows]
    both = lambda name, fn: jnp.stack([fn(piece(name + '0')), fn(piece(name + '1'))])
    ident, transpose = (lambda a: a), (lambda a: a.T)
    grads = {'w_in_ab': piece('w_in_t', W_IN_PIECE).T[None], 'w_glu_b': piece('w_glu').reshape(inp['w_glu_b'].shape),
             'w_out_ab': piece('w_out')[None], 'pool_w': piece('pool_w').reshape(inp['pool_w'].shape),
             'xa_wq': both('wq', ident), 'xa_wkv': both('wkv_t', transpose), 'xa_wo': both('wo', ident),
             'ffn_w_up': both('up_t', transpose), 'ffn_w_down': both('down', ident)}
    misc = _unpack(misc_sum, [inp[n].shape for n in REP_NAMES] + [SMALL_SHARDED[n][0] for n in SMALL_NAMES])
    for n, g in zip(REP_NAMES, misc):
        grads[n] = g
    for n, g in zip(SMALL_NAMES, misc[len(REP_NAMES):]):
        grads[n] = lax.dynamic_index_in_dim(_split_shards(g, SMALL_SHARDED[n][1]), dev, 0, keepdims=False
                                            ).reshape(inp[n].shape)
    upd = {}
    for n in BIG_NAMES:
        upd[n] = _adamw(inp[n], grads[n], inp['m_' + n], inp['v_' + n], "adamw_" + n)
    tiny_names = REP_NAMES + SMALL_NAMES
    rep_total = sum(int(np.prod(inp[n].shape)) for n in REP_NAMES)
    packs = [_pack([inp[prefix + n] for n in tiny_names]) for prefix in ('', 'm_', 'v_')]
    g_pack = _pack([misc_sum.reshape(-1)[:rep_total]] + [grads[n] for n in SMALL_NAMES])
    tiny_out = [_unpack(o, [inp[n].shape for n in tiny_names])
                for o in _adamw(packs[0], g_pack, packs[1], packs[2], "adamw_small")]
    for i, n in enumerate(tiny_names):
        upd[n] = tuple(o[i] for o in tiny_out)

    loss = lax.psum(loss_part[0, 0], ("x", "y", "c"))
    outs = [loss, grad_x[None]]
    outs += [grads[n] for n in WEIGHT_NAMES]
    for i in range(3):
        outs += [upd[n][i] for n in WEIGHT_NAMES]
    return tuple(outs)


def _step(inp):
    loss_part, grad_x, big_reduced, rep_grads, small_grads = _local_step(inp)
    misc_sum = _reduce_small_gradients(rep_grads, small_grads)
    return _update(inp, loss_part, grad_x, big_reduced, misc_sum)


INPUT_NAMES = (['x', 'mem'] + WEIGHT_NAMES + ['loss_target'] + ['m_' + n for n in WEIGHT_NAMES]
               + ['v_' + n for n in WEIGHT_NAMES])


def kernel(x, mem, norm_mix_g, norm_xa_g, norm_ffn_g, norm_mem_g, norm_final_g, w_in_ab, conv_qkv_a, a_log_a, dt_bias_a, onorm_g_a, ssm_lambda_re, ssm_lambda_im, ssm_b_re, ssm_b_im, ssm_c_re, ssm_c_im, ssm_d, ssm_log_dt, w_glu_b, b_glu_b, w_out_ab, pool_w, pool_scale, xa_wq, xa_wkv, xa_wo, ffn_w_up, ffn_conv, ffn_w_down, loss_target, m_norm_mix_g, m_norm_xa_g, m_norm_ffn_g, m_norm_mem_g, m_norm_final_g, m_w_in_ab, m_conv_qkv_a, m_a_log_a, m_dt_bias_a, m_onorm_g_a, m_ssm_lambda_re, m_ssm_lambda_im, m_ssm_b_re, m_ssm_b_im, m_ssm_c_re, m_ssm_c_im, m_ssm_d, m_ssm_log_dt, m_w_glu_b, m_b_glu_b, m_w_out_ab, m_pool_w, m_pool_scale, m_xa_wq, m_xa_wkv, m_xa_wo, m_ffn_w_up, m_ffn_conv, m_ffn_w_down, v_norm_mix_g, v_norm_xa_g, v_norm_ffn_g, v_norm_mem_g, v_norm_final_g, v_w_in_ab, v_conv_qkv_a, v_a_log_a, v_dt_bias_a, v_onorm_g_a, v_ssm_lambda_re, v_ssm_lambda_im, v_ssm_b_re, v_ssm_b_im, v_ssm_c_re, v_ssm_c_im, v_ssm_d, v_ssm_log_dt, v_w_glu_b, v_b_glu_b, v_w_out_ab, v_pool_w, v_pool_scale, v_xa_wq, v_xa_wkv, v_xa_wo, v_ffn_w_up, v_ffn_conv, v_ffn_w_down):
    args = (x, mem, norm_mix_g, norm_xa_g, norm_ffn_g, norm_mem_g, norm_final_g, w_in_ab, conv_qkv_a, a_log_a, dt_bias_a, onorm_g_a, ssm_lambda_re, ssm_lambda_im, ssm_b_re, ssm_b_im, ssm_c_re, ssm_c_im, ssm_d, ssm_log_dt, w_glu_b, b_glu_b, w_out_ab, pool_w, pool_scale, xa_wq, xa_wkv, xa_wo, ffn_w_up, ffn_conv, ffn_w_down, loss_target, m_norm_mix_g, m_norm_xa_g, m_norm_ffn_g, m_norm_mem_g, m_norm_final_g, m_w_in_ab, m_conv_qkv_a, m_a_log_a, m_dt_bias_a, m_onorm_g_a, m_ssm_lambda_re, m_ssm_lambda_im, m_ssm_b_re, m_ssm_b_im, m_ssm_c_re, m_ssm_c_im, m_ssm_d, m_ssm_log_dt, m_w_glu_b, m_b_glu_b, m_w_out_ab, m_pool_w, m_pool_scale, m_xa_wq, m_xa_wkv, m_xa_wo, m_ffn_w_up, m_ffn_conv, m_ffn_w_down, v_norm_mix_g, v_norm_xa_g, v_norm_ffn_g, v_norm_mem_g, v_norm_final_g, v_w_in_ab, v_conv_qkv_a, v_a_log_a, v_dt_bias_a, v_onorm_g_a, v_ssm_lambda_re, v_ssm_lambda_im, v_ssm_b_re, v_ssm_b_im, v_ssm_c_re, v_ssm_c_im, v_ssm_d, v_ssm_log_dt, v_w_glu_b, v_b_glu_b, v_w_out_ab, v_pool_w, v_pool_scale, v_xa_wq, v_xa_wkv, v_xa_wo, v_ffn_w_up, v_ffn_conv, v_ffn_w_down)
    return _step(dict(zip(INPUT_NAMES, args)))
```

```python
import functools
import math

import numpy as np
import jax
import jax.numpy as jnp
from jax import lax
from jax.experimental import pallas as pl
from jax.experimental.pallas import tpu as pltpu

F32, BF16 = jnp.float32, jnp.bfloat16
HIGH, HIGHEST = lax.Precision.HIGH, lax.Precision.HIGHEST
MESH = pl.DeviceIdType.MESH

N_DEV = 8
SEQ, D_MODEL, MEM_LEN = 2048, 1024, 256
WIDTH_A, N_HEADS_A, HEAD_A, CONV_A = 512, 4, 128, 4
GDR_CHUNK = 128
GDR_HEADS_PER_STEP = 4
SSM_WIDTH, SSM_GROUP, N_GROUPS, SSM_STATE = 512, 16, 32, 64
SSM_CH = N_GROUPS * SSM_STATE
SCAN_CB = 512
POOL_WINDOWS = (2, 4, 8, 16)
POOL_GROUP = 256
N_HEADS_X, HEAD_X = 4, 256
D_FF, CONV_FFN = 2816, 3
RMS_EPS = 1e-6
ADAM_LR, ADAM_B1, ADAM_B2, ADAM_EPS, ADAM_WD, ADAM_STEP = 0.001, 0.9, 0.999, 1e-08, 0.01, 10
LANE = 128
PACK_COLS = 1024
VMEM_LIMIT_BYTES = 56 * 1024 * 1024


def _params(sem=None):
    return pltpu.CompilerParams(dimension_semantics=sem, vmem_limit_bytes=VMEM_LIMIT_BYTES)


class Comm:
    def __init__(self, inputs, out_shapes, sems, start, end, mid=None):
        self.inputs, self.out_shapes, self.sems = list(inputs), list(out_shapes), list(sems)
        self.start, self.mid, self.end = start, mid, end


def _merge_comms(comms):
    comms = [c for c in comms if c is not None]
    if not comms:
        return None, []
    bounds, ni, no, ns = [], 0, 0, 0
    for c in comms:
        bounds.append((ni, no, ns))
        ni, no, ns = ni + len(c.inputs), no + len(c.out_shapes), ns + len(c.sems)

    def phase(which):
        def run(ins, outs, sems):
            for c, (i0, o0, s0) in zip(comms, bounds):
                fn = getattr(c, which)
                if fn is not None:
                    fn(ins[i0:i0 + len(c.inputs)], outs[o0:o0 + len(c.out_shapes)], sems[s0:s0 + len(c.sems)])
        return run

    merged = Comm([a for c in comms for a in c.inputs], [s for c in comms for s in c.out_shapes],
                  [s for c in comms for s in c.sems], phase("start"), phase("end"), phase("mid"))
    return merged, [(o0, o0 + len(c.out_shapes)) for c, (_, o0, _) in zip(comms, bounds)]


def _call(body, *, name, grid, in_specs, out_specs, out_shape, args, scratch_shapes=(), sem=None, comm=None):
    single = not isinstance(out_shape, (list, tuple))
    out_specs_l = [out_specs] if single else list(out_specs)
    out_shape_l = [out_shape] if single else list(out_shape)
    scratch_shapes = list(scratch_shapes)
    merged, spans = _merge_comms(comm if isinstance(comm, (list, tuple)) else [comm])
    if merged is None:
        outs = pl.pallas_call(body, name=name, grid=grid, in_specs=list(in_specs), out_specs=out_specs_l,
                              out_shape=out_shape_l, scratch_shapes=scratch_shapes, compiler_params=_params(sem))(*args)
        outs = outs[0] if single else outs
        return outs if comm is None else (outs, [])
    n_in, n_out, n_scr = len(in_specs), len(out_specs_l), len(scratch_shapes)
    ci, co = len(merged.inputs), len(merged.out_shapes)
    total = int(np.prod(grid))

    def wrapped(*refs):
        ins, cins = refs[:n_in], refs[n_in:n_in + ci]
        outs, couts = refs[n_in + ci:n_in + ci + n_out], refs[n_in + ci + n_out:n_in + ci + n_out + co]
        scr, csems = refs[n_in + ci + n_out + co:n_in + ci + n_out + co + n_scr], refs[n_in + ci + n_out + co + n_scr:]
        lin = pl.program_id(0)
        for d in range(1, len(grid)):
            lin = lin * grid[d] + pl.program_id(d)
        pl.when(lin == 0)(lambda: merged.start(cins, couts, csems))
        body(*ins, *outs, *scr)
        mid_step = min((3 * total) // 4, total - 1)
        pl.when(lin == mid_step)(lambda: merged.mid(cins, couts, csems))
        pl.when(lin == total - 1)(lambda: merged.end(cins, couts, csems))

    any_spec = pl.BlockSpec(memory_space=pl.ANY)
    res = pl.pallas_call(
        wrapped, name=name, grid=grid, in_specs=list(in_specs) + [any_spec] * ci,
        out_specs=out_specs_l + [any_spec] * co, out_shape=out_shape_l + merged.out_shapes,
        scratch_shapes=scratch_shapes + merged.sems,
        compiler_params=_params(("arbitrary",) * len(grid)))(*args, *merged.inputs)
    outs, couts = res[:n_out], res[n_out:]
    return (outs[0] if single else list(outs)), [list(couts[a:b]) for a, b in spans]


def _comm_only(comm, name):
    def body():
        pass

    _, couts = _call(body, name=name, grid=(1,), in_specs=[], out_specs=[], out_shape=[], args=[], comm=comm)
    return couts[0]


def _tile(dim, pref):
    best = None
    for t in range(LANE, min(dim, pref) + 1, LANE):
        if dim % t == 0:
            best = t
    return best if best is not None else dim


MM_VMEM_BUDGET = 40 * 1024 * 1024


def _mm_tiles(m, n, k, a_bytes, b_bytes, o_bytes, r_bytes):
    for tk in (k, _tile(k, 2048), _tile(k, 1024), _tile(k, 512)):
        for tm, tn in ((1024, 1536), (1024, 1024), (1024, 512), (512, 512), (256, 512), (256, 256)):
            tm, tn = _tile(m, tm), _tile(n, tn)
            acc = 0 if tk == k else tm * tn * 4
            need = 2 * (tm * tk * a_bytes + tk * tn * b_bytes + tm * tn * (o_bytes + r_bytes)) + acc
            if need <= MM_VMEM_BUDGET:
                return tm, tn, tk
    raise ValueError("no matmul tiling fits VMEM")


def _mm(a, b, mode, name, out_dtype=F32, res=None, comm=None):
    if mode == "nn":
        (m, k), n = a.shape, b.shape[1]
    elif mode == "nt":
        (m, k), n = a.shape, b.shape[0]
    else:
        (k, m), n = a.shape, b.shape[1]
    tm, tn, tk = _mm_tiles(m, n, k, a.dtype.itemsize, b.dtype.itemsize, jnp.dtype(out_dtype).itemsize,
                           0 if res is None else res.dtype.itemsize)
    nk = k // tk
    dims = {"nn": ((1,), (0,)), "nt": ((1,), (1,)), "tn": ((0,), (0,))}[mode]

    def body(*refs):
        if res is None:
            a_ref, b_ref, o_ref = refs[:3]
            r_ref = None
        else:
            a_ref, b_ref, r_ref, o_ref = refs[:4]
        part = lax.dot_general(a_ref[...].astype(BF16), b_ref[...].astype(BF16), (dims, ((), ())),
                               preferred_element_type=F32)

        def finish(out):
            if r_ref is not None:
                out = out + r_ref[...].astype(F32)
            o_ref[...] = out.astype(out_dtype)

        if nk == 1:
            finish(part)
            return
        acc = refs[-1]
        kk = pl.program_id(2)

        @pl.when(kk == 0)
        def _():
            acc[...] = part

        @pl.when(kk > 0)
        def _():
            acc[...] += part

        @pl.when(kk == nk - 1)
        def _():
            finish(acc[...])

    a_spec = (pl.BlockSpec((tk, tm), lambda i, j, q: (q, i)) if mode == "tn"
              else pl.BlockSpec((tm, tk), lambda i, j, q: (i, q)))
    b_spec = (pl.BlockSpec((tn, tk), lambda i, j, q: (j, q)) if mode == "nt"
              else pl.BlockSpec((tk, tn), lambda i, j, q: (q, j)))
    o_spec = pl.BlockSpec((tm, tn), lambda i, j, q: (i, j))
    in_specs, args = [a_spec, b_spec], [a, b]
    if res is not None:
        in_specs.append(o_spec)
        args.append(res)
    return _call(body, name=name, grid=(m // tm, n // tn, nk), in_specs=in_specs, out_specs=o_spec,
                 out_shape=jax.ShapeDtypeStruct((m, n), out_dtype),
                 scratch_shapes=[] if nk == 1 else [pltpu.VMEM((tm, tn), F32)],
                 sem=("parallel", "parallel", "arbitrary"), args=args, comm=comm)


def _mm_bd(a, b, mode, name, out_dtype=F32, res=None, comm=None, tm=1024):
    if mode == "tn":
        k = a.shape[0]
        nb = min(a.shape[1], b.shape[1]) // LANE
        ma, n = a.shape[1] // nb, b.shape[1] // nb

        def body(a_ref, b_ref, o_ref):
            o_ref[0] = lax.dot_general(a_ref[...].astype(BF16), b_ref[...].astype(BF16), (((0,), (0,)), ((), ())),
                                       preferred_element_type=F32).astype(out_dtype)

        return _call(body, name=name, grid=(nb,),
                     in_specs=[pl.BlockSpec((k, ma), lambda j: (0, j)), pl.BlockSpec((k, n), lambda j: (0, j))],
                     out_specs=pl.BlockSpec((1, ma, n), lambda j: (j, 0, 0)),
                     out_shape=jax.ShapeDtypeStruct((nb, ma, n), out_dtype), sem=("parallel",), args=(a, b), comm=comm)
    m = a.shape[0]
    nb = b.shape[0]
    ka = a.shape[1] // nb
    n = b.shape[2] if mode == "nn" else b.shape[1]
    tm = _tile(m, tm)
    dims = ((1,), (0,)) if mode == "nn" else ((1,), (1,))

    def body(*refs):
        if res is None:
            a_ref, b_ref, o_ref = refs
            r_ref = None
        else:
            a_ref, b_ref, r_ref, o_ref = refs
        out = lax.dot_general(a_ref[...].astype(BF16), b_ref[0].astype(BF16), (dims, ((), ())),
                              preferred_element_type=F32)
        if r_ref is not None:
            out = out + r_ref[...].astype(F32)
        o_ref[...] = out.astype(out_dtype)

    o_spec = pl.BlockSpec((tm, n), lambda i, j: (i, j))
    in_specs = [pl.BlockSpec((tm, ka), lambda i, j: (i, j)), pl.BlockSpec((1,) + b.shape[1:], lambda i, j: (j, 0, 0))]
    args = [a, b]
    if res is not None:
        in_specs.append(o_spec)
        args.append(res)
    return _call(body, name=name, grid=(m // tm, nb), in_specs=in_specs, out_specs=o_spec,
                 out_shape=jax.ShapeDtypeStruct((m, nb * n), out_dtype), sem=("parallel", "parallel"),
                 args=args, comm=comm)


def _rms_fwd(x, g, out_dtype, name, tr=256):
    rows, d = x.shape

    def body(x_ref, g_ref, o_ref):
        xv = x_ref[...]
        r = lax.rsqrt(jnp.mean(xv * xv, axis=-1, keepdims=True) + RMS_EPS)
        o_ref[...] = (xv * r * g_ref[...]).astype(out_dtype)

    return pl.pallas_call(
        body, name=name, grid=(rows // tr,),
        in_specs=[pl.BlockSpec((tr, d), lambda i: (i, 0)), pl.BlockSpec((1, d), lambda i: (0, 0))],
        out_specs=pl.BlockSpec((tr, d), lambda i: (i, 0)), out_shape=jax.ShapeDtypeStruct((rows, d), out_dtype),
        compiler_params=_params(("parallel",)))(x, g.reshape(1, d))


def _rms_bwd(x, g, dy, dres, name, tr=256, comm=None):
    rows, d = x.shape

    def body(*refs):
        if dres is None:
            x_ref, g_ref, dy_ref, dx_ref, dg_ref = refs
            r_ref = None
        else:
            x_ref, g_ref, dy_ref, r_ref, dx_ref, dg_ref = refs

        @pl.when(pl.program_id(0) == 0)
        def _():
            dg_ref[...] = jnp.zeros_like(dg_ref)

        xv, dyv = x_ref[...], dy_ref[...].astype(F32)
        r = lax.rsqrt(jnp.mean(xv * xv, axis=-1, keepdims=True) + RMS_EPS)
        xh = xv * r
        dyg = dyv * g_ref[...]
        dx = r * (dyg - xh * jnp.mean(dyg * xh, axis=-1, keepdims=True))
        if r_ref is not None:
            dx = dx + r_ref[...]
        dx_ref[...] = dx
        dg_ref[...] += jnp.sum(dyv * xh, axis=0, keepdims=True)

    blk = pl.BlockSpec((tr, d), lambda i: (i, 0))
    vec = pl.BlockSpec((1, d), lambda i: (0, 0))
    in_specs, args = [blk, vec, blk], [x, g.reshape(1, d), dy]
    if dres is not None:
        in_specs.append(blk)
        args.append(dres)
    return _call(
        body, name=name, grid=(rows // tr,), in_specs=in_specs, out_specs=[blk, vec],
        out_shape=[jax.ShapeDtypeStruct((rows, d), F32), jax.ShapeDtypeStruct((1, d), F32)],
        sem=("arbitrary",), args=args, comm=comm)


def _loss_head(x, g, target, name, tr=256):
    rows, d = x.shape

    def body(x_ref, g_ref, t_ref, loss_ref, dx_ref, dg_ref):
        @pl.when(pl.program_id(0) == 0)
        def _():
            dg_ref[...] = jnp.zeros_like(dg_ref)
            loss_ref[...] = jnp.zeros_like(loss_ref)

        xv = x_ref[...]
        r = lax.rsqrt(jnp.mean(xv * xv, axis=-1, keepdims=True) + RMS_EPS)
        xh = xv * r
        err = xh * g_ref[...] - t_ref[...]
        loss_ref[...] += 0.5 * jnp.sum(jnp.mean(err * err, axis=-1, keepdims=True), keepdims=True)
        dyv = err * (1.0 / d)
        dyg = dyv * g_ref[...]
        dx_ref[...] = r * (dyg - xh * jnp.mean(dyg * xh, axis=-1, keepdims=True))
        dg_ref[...] += jnp.sum(dyv * xh, axis=0, keepdims=True)

    blk = pl.BlockSpec((tr, d), lambda i: (i, 0))
    vec = pl.BlockSpec((1, d), lambda i: (0, 0))
    return pl.pallas_call(
        body, name=name, grid=(rows // tr,), in_specs=[blk, vec, blk],
        out_specs=[pl.BlockSpec((1, 1), lambda i: (0, 0)), blk, vec],
        out_shape=[jax.ShapeDtypeStruct((1, 1), F32), jax.ShapeDtypeStruct((rows, d), F32),
                   jax.ShapeDtypeStruct((1, d), F32)],
        compiler_params=_params(("arbitrary",)))(x, g.reshape(1, d), target)


def _shift_down(x, s):
    rows = lax.broadcasted_iota(jnp.int32, x.shape, 0)
    return jnp.where(rows >= s, pltpu.roll(x, s, 0), 0.0)


def _shift_up(x, s):
    n = x.shape[0]
    rows = lax.broadcasted_iota(jnp.int32, x.shape, 0)
    return jnp.where(rows < n - s, pltpu.roll(x, n - s, 0), 0.0)


def _sigmoid(x):
    return 1.0 / (1.0 + jnp.exp(-x))


def _silu_and_grad(x):
    s = _sigmoid(x)
    return x * s, s * (1.0 + x * (1.0 - s))


_GELU_C0, _GELU_C1 = math.sqrt(2.0 / math.pi), 0.044715


def _gelu_and_grad(x):
    th = jnp.tanh(_GELU_C0 * (x + _GELU_C1 * x * x * x))
    y = 0.5 * x * (1.0 + th)
    dy = 0.5 * (1.0 + th) + 0.5 * x * (1.0 - th * th) * _GELU_C0 * (1.0 + 3.0 * _GELU_C1 * x * x)
    return y, dy


def _ffn_act_fwd(h, w, name, tc=256, comm=None):
    t = h.shape[0]
    nb = D_FF // tc

    def body(hg_ref, hv_ref, wg_ref, wv_ref, a_ref):
        def conv(x, wr):
            return wr[2:3, :] * x + wr[1:2, :] * _shift_down(x, 1) + wr[0:1, :] * _shift_down(x, 2)

        cg = conv(hg_ref[...], wg_ref[...])
        cv = conv(hv_ref[...], wv_ref[...])
        a_ref[...] = (cg * _sigmoid(cg) * cv).astype(BF16)

    return _call(
        body, name=name, grid=(nb,),
        in_specs=[pl.BlockSpec((t, tc), lambda j: (0, j)), pl.BlockSpec((t, tc), lambda j: (0, j + nb)),
                  pl.BlockSpec((CONV_FFN, tc), lambda j: (0, j)), pl.BlockSpec((CONV_FFN, tc), lambda j: (0, j + nb))],
        out_specs=pl.BlockSpec((t, tc), lambda j: (0, j)), out_shape=jax.ShapeDtypeStruct((t, D_FF), BF16),
        sem=("parallel",), args=(h, h, w, w), comm=comm)


def _ffn_act_bwd(h, w, da, name, tc=256, comm=None):
    t = h.shape[0]
    nb = D_FF // tc

    def body(hg_ref, hv_ref, wg_ref, wv_ref, da_ref, dhg_ref, dhv_ref, dwg_ref, dwv_ref):
        hg, hv, wg, wv = hg_ref[...], hv_ref[...], wg_ref[...], wv_ref[...]
        hg1, hg2, hv1, hv2 = _shift_down(hg, 1), _shift_down(hg, 2), _shift_down(hv, 1), _shift_down(hv, 2)
        cg = wg[2:3, :] * hg + wg[1:2, :] * hg1 + wg[0:1, :] * hg2
        cv = wv[2:3, :] * hv + wv[1:2, :] * hv1 + wv[0:1, :] * hv2
        sg, dsg = _silu_and_grad(cg)
        dav = da_ref[...].astype(F32)
        dcv = dav * sg
        dcg = dav * cv * dsg

        def conv_t(dc, wr):
            return wr[2:3, :] * dc + wr[1:2, :] * _shift_up(dc, 1) + wr[0:1, :] * _shift_up(dc, 2)

        dhg_ref[...] = conv_t(dcg, wg).astype(BF16)
        dhv_ref[...] = conv_t(dcv, wv).astype(BF16)
        dwg_ref[0:1, :] = jnp.sum(dcg * hg2, axis=0, keepdims=True)
        dwg_ref[1:2, :] = jnp.sum(dcg * hg1, axis=0, keepdims=True)
        dwg_ref[2:3, :] = jnp.sum(dcg * hg, axis=0, keepdims=True)
        dwv_ref[0:1, :] = jnp.sum(dcv * hv2, axis=0, keepdims=True)
        dwv_ref[1:2, :] = jnp.sum(dcv * hv1, axis=0, keepdims=True)
        dwv_ref[2:3, :] = jnp.sum(dcv * hv, axis=0, keepdims=True)

    big = lambda off: pl.BlockSpec((t, tc), lambda j: (0, j + off))
    small = lambda off: pl.BlockSpec((CONV_FFN, tc), lambda j: (0, j + off))
    res = _call(
        body, name=name, grid=(nb,),
        in_specs=[big(0), big(nb), small(0), small(nb), big(0)],
        out_specs=[big(0), big(0), small(0), small(0)],
        out_shape=[jax.ShapeDtypeStruct((t, D_FF), BF16), jax.ShapeDtypeStruct((t, D_FF), BF16),
                   jax.ShapeDtypeStruct((CONV_FFN, D_FF), F32), jax.ShapeDtypeStruct((CONV_FFN, D_FF), F32)],
        sem=("parallel",), args=(h, h, w, w, da), comm=comm)
    (dhg, dhv, dwg, dwv), couts = res if comm is not None else (res, None)
    out = (jnp.concatenate([dhg, dhv], axis=1), jnp.concatenate([dwg, dwv], axis=1))
    return out if comm is None else (out, couts)


def _attn_probs(q, k):
    s = lax.dot_general(q.astype(BF16), k.astype(BF16), (((1,), (1,)), ((), ())),
                        preferred_element_type=F32) * (HEAD_X ** -0.5)
    s = s - jnp.max(s, axis=-1, keepdims=True)
    p = jnp.exp(s)
    return p / jnp.sum(p, axis=-1, keepdims=True)


def _attn_fwd(q, kv, name, tq=512, comm=None):
    t = q.shape[0]

    def body(q_ref, k_ref, v_ref, o_ref):
        p = _attn_probs(q_ref[...], k_ref[...])
        o_ref[...] = jnp.dot(p.astype(BF16), v_ref[...].astype(BF16), preferred_element_type=F32).astype(BF16)

    return _call(
        body, name=name, grid=(N_HEADS_X, t // tq),
        in_specs=[pl.BlockSpec((tq, HEAD_X), lambda h, i: (i, h)),
                  pl.BlockSpec((MEM_LEN, HEAD_X), lambda h, i: (0, h)),
                  pl.BlockSpec((MEM_LEN, HEAD_X), lambda h, i: (0, h + N_HEADS_X))],
        out_specs=pl.BlockSpec((tq, HEAD_X), lambda h, i: (i, h)),
        out_shape=jax.ShapeDtypeStruct((t, N_HEADS_X * HEAD_X), BF16),
        sem=("parallel", "parallel"), args=(q, kv, kv), comm=comm)


def _attn_bwd(q, kv, do, name, tq=512):
    t = q.shape[0]

    def body(q_ref, k_ref, v_ref, do_ref, dq_ref, dk_ref, dv_ref):
        @pl.when(pl.program_id(1) == 0)
        def _():
            dk_ref[...] = jnp.zeros_like(dk_ref)
            dv_ref[...] = jnp.zeros_like(dv_ref)

        qb, kb, vb, dob = (r[...].astype(BF16) for r in (q_ref, k_ref, v_ref, do_ref))
        p = _attn_probs(qb, kb)
        dp = lax.dot_general(dob, vb, (((1,), (1,)), ((), ())), preferred_element_type=F32)
        ds = p * (dp - jnp.sum(dp * p, axis=-1, keepdims=True)) * (HEAD_X ** -0.5)
        dsb = ds.astype(BF16)
        dq_ref[...] = jnp.dot(dsb, kb, preferred_element_type=F32).astype(BF16)
        dk_ref[...] += lax.dot_general(dsb, qb, (((0,), (0,)), ((), ())), preferred_element_type=F32)
        dv_ref[...] += lax.dot_general(p.astype(BF16), dob, (((0,), (0,)), ((), ())), preferred_element_type=F32)

    qs = pl.BlockSpec((tq, HEAD_X), lambda h, i: (i, h))
    ms = pl.BlockSpec((MEM_LEN, HEAD_X), lambda h, i: (0, h))
    return pl.pallas_call(
        body, name=name, grid=(N_HEADS_X, t // tq),
        in_specs=[qs, ms, pl.BlockSpec((MEM_LEN, HEAD_X), lambda h, i: (0, h + N_HEADS_X)), qs],
        out_specs=[qs, ms, ms],
        out_shape=[jax.ShapeDtypeStruct((t, D_MODEL), BF16), jax.ShapeDtypeStruct((MEM_LEN, D_MODEL), F32),
                   jax.ShapeDtypeStruct((MEM_LEN, D_MODEL), F32)],
        compiler_params=_params(("parallel", "arbitrary")))(q, kv, kv, do)


def _pool_counts(t, win):
    pos = lax.broadcasted_iota(jnp.int32, (t, 1), 0).astype(F32) + 1.0
    return 1.0 / jnp.minimum(pos, float(win))


def _pool_delta(xv, win):
    s, step = xv, 1
    while step < win:
        s = s + _shift_down(s, step)
        step *= 2
    return s * _pool_counts(xv.shape[0], win) - xv


def _pool_delta_t(dv, win):
    s, step = dv * _pool_counts(dv.shape[0], win), 1
    while step < win:
        s = s + _shift_up(s, step)
        step *= 2
    return s - dv


def _pool_fwd(xn, w, scale, res, name):
    t = xn.shape[0]

    def make_branch(win, xn_ref, w_ref, s_ref, r_ref, o_ref):
        def branch():
            dl = _pool_delta(xn_ref[...], win)
            y = jnp.dot(dl.astype(BF16), w_ref[0], preferred_element_type=F32)
            o_ref[...] = r_ref[...] + y * s_ref[...]
        return branch

    def body(xn_ref, w_ref, s_ref, r_ref, o_ref):
        for gi, win in enumerate(POOL_WINDOWS):
            pl.when(pl.program_id(0) == gi)(make_branch(win, xn_ref, w_ref, s_ref, r_ref, o_ref))

    blk = pl.BlockSpec((t, POOL_GROUP), lambda g: (0, g))
    return pl.pallas_call(
        body, name=name, grid=(len(POOL_WINDOWS),),
        in_specs=[blk, pl.BlockSpec((1, POOL_GROUP, POOL_GROUP), lambda g: (g, 0, 0)),
                  pl.BlockSpec((1, POOL_GROUP), lambda g: (0, g)), blk],
        out_specs=blk, out_shape=jax.ShapeDtypeStruct((t, D_MODEL), F32),
        compiler_params=_params(("parallel",)))(xn, w, scale, res)


def _pool_bwd(xn, w, scale, dmix, name):
    t = xn.shape[0]

    def make_branch(win, xn_ref, w_ref, s_ref, d_ref, dxn_ref, dw_ref, ds_ref):
        def branch():
            dl = _pool_delta(xn_ref[...], win).astype(BF16)
            wv = w_ref[0]
            dm = d_ref[...]
            y = jnp.dot(dl, wv, preferred_element_type=F32)
            ds_ref[...] = jnp.sum(dm * y, axis=0, keepdims=True)
            dy = (dm * s_ref[...]).astype(BF16)
            dw_ref[0] = lax.dot_general(dl, dy, (((0,), (0,)), ((), ())), preferred_element_type=F32)
            ddl = lax.dot_general(dy, wv, (((1,), (1,)), ((), ())), preferred_element_type=F32)
            dxn_ref[...] = _pool_delta_t(ddl, win)
        return branch

    def body(*refs):
        for gi, win in enumerate(POOL_WINDOWS):
            pl.when(pl.program_id(0) == gi)(make_branch(win, *refs))

    blk = pl.BlockSpec((t, POOL_GROUP), lambda g: (0, g))
    wspec = pl.BlockSpec((1, POOL_GROUP, POOL_GROUP), lambda g: (g, 0, 0))
    vec = pl.BlockSpec((1, POOL_GROUP), lambda g: (0, g))
    return pl.pallas_call(
        body, name=name, grid=(len(POOL_WINDOWS),), in_specs=[blk, wspec, vec, blk], out_specs=[blk, wspec, vec],
        out_shape=[jax.ShapeDtypeStruct((t, D_MODEL), F32),
                   jax.ShapeDtypeStruct((len(POOL_WINDOWS), POOL_GROUP, POOL_GROUP), F32),
                   jax.ShapeDtypeStruct((1, D_MODEL), F32)],
        compiler_params=_params(("parallel",)))(xn, w, scale, dmix)


def _qkv_conv(h, wr):
    return (wr[3:4, :] * h + wr[2:3, :] * _shift_down(h, 1) + wr[1:2, :] * _shift_down(h, 2)
            + wr[0:1, :] * _shift_down(h, 3))


def _qkv_pre_fwd(h, w, col0, ncols, normalize, scale, name):
    t = h.shape[0]

    def body(h_ref, w_ref, o_ref):
        c = _qkv_conv(h_ref[...], w_ref[...])
        s = c * _sigmoid(c)
        if normalize:
            s = s * lax.rsqrt(jnp.sum(s * s, axis=-1, keepdims=True) + 1e-6) * scale
        o_ref[...] = s

    return pl.pallas_call(
        body, name=name, grid=(ncols,),
        in_specs=[pl.BlockSpec((t, HEAD_A), lambda j: (0, j + col0)), pl.BlockSpec((CONV_A, HEAD_A), lambda j: (0, j + col0))],
        out_specs=pl.BlockSpec((t, HEAD_A), lambda j: (0, j)), out_shape=jax.ShapeDtypeStruct((t, ncols * HEAD_A), F32),
        compiler_params=_params(("parallel",)))(h, w)


def _qkv_pre_bwd(h, w, dy, col0, ncols, normalize, scale, name):
    t = h.shape[0]

    def body(h_ref, w_ref, dy_ref, dh_ref, dw_ref):
        hv, wr, dyv = h_ref[...], w_ref[...], dy_ref[...]
        h1, h2, h3 = _shift_down(hv, 1), _shift_down(hv, 2), _shift_down(hv, 3)
        c = wr[3:4, :] * hv + wr[2:3, :] * h1 + wr[1:2, :] * h2 + wr[0:1, :] * h3
        s, dsilu = _silu_and_grad(c)
        if normalize:
            r = lax.rsqrt(jnp.sum(s * s, axis=-1, keepdims=True) + 1e-6)
            y = s * r
            dyv = dyv * scale
            ds = r * (dyv - y * jnp.sum(dyv * y, axis=-1, keepdims=True))
        else:
            ds = dyv
        dc = ds * dsilu
        dh = (wr[3:4, :] * dc + wr[2:3, :] * _shift_up(dc, 1) + wr[1:2, :] * _shift_up(dc, 2)
              + wr[0:1, :] * _shift_up(dc, 3))
        dh_ref[...] = dh.astype(BF16)
        dw_ref[0:1, :] = jnp.sum(dc * h3, axis=0, keepdims=True)
        dw_ref[1:2, :] = jnp.sum(dc * h2, axis=0, keepdims=True)
        dw_ref[2:3, :] = jnp.sum(dc * h1, axis=0, keepdims=True)
        dw_ref[3:4, :] = jnp.sum(dc * hv, axis=0, keepdims=True)

    return pl.pallas_call(
        body, name=name, grid=(ncols,),
        in_specs=[pl.BlockSpec((t, HEAD_A), lambda j: (0, j + col0)), pl.BlockSpec((CONV_A, HEAD_A), lambda j: (0, j + col0)),
                  pl.BlockSpec((t, HEAD_A), lambda j: (0, j))],
        out_specs=[pl.BlockSpec((t, HEAD_A), lambda j: (0, j)), pl.BlockSpec((CONV_A, HEAD_A), lambda j: (0, j))],
        out_shape=[jax.ShapeDtypeStruct((t, ncols * HEAD_A), BF16), jax.ShapeDtypeStruct((CONV_A, ncols * HEAD_A), F32)],
        compiler_params=_params(("parallel",)))(h, w, dy)


def _softplus(x):
    return jnp.maximum(x, 0.0) + jnp.log1p(jnp.exp(-jnp.abs(x)))


def _gates_fwd(ba, arow, brow, name):
    t = ba.shape[0]

    def body(x_ref, a_ref, b_ref, o_ref):
        xv = x_ref[...]
        lane = lax.broadcasted_iota(jnp.int32, xv.shape, 1)
        beta = _sigmoid(xv)
        g = -jnp.exp(a_ref[...]) * _softplus(xv + b_ref[...])
        o_ref[...] = jnp.where(lane < N_HEADS_A, beta, jnp.where(lane < 2 * N_HEADS_A, g, 0.0))

    return pl.pallas_call(body, name=name, out_shape=jax.ShapeDtypeStruct((t, LANE), F32),
                          compiler_params=_params())(ba, arow, brow)


def _gates_bwd(ba, arow, brow, dgb, name):
    t = ba.shape[0]

    def body(x_ref, a_ref, b_ref, d_ref, dx_ref, da_ref, db_ref):
        xv = x_ref[...]
        dv = d_ref[0] + d_ref[1] + d_ref[2] + d_ref[3]
        lane = lax.broadcasted_iota(jnp.int32, xv.shape, 1)
        beta = _sigmoid(xv)
        ea = jnp.exp(a_ref[...])
        z = xv + b_ref[...]
        dgv = jnp.where((lane >= N_HEADS_A) & (lane < 2 * N_HEADS_A), dv, 0.0) * (-ea)
        dz = dgv * _sigmoid(z)
        dx = jnp.where(lane < N_HEADS_A, dv * beta * (1.0 - beta), dz)
        dx_ref[...] = dx.astype(BF16)
        db_ref[...] = jnp.sum(dz, axis=0, keepdims=True)
        da_ref[...] = jnp.sum(dgv * _softplus(z), axis=0, keepdims=True)

    return pl.pallas_call(
        body, name=name,
        out_shape=[jax.ShapeDtypeStruct((t, LANE), BF16), jax.ShapeDtypeStruct((1, LANE), F32),
                   jax.ShapeDtypeStruct((1, LANE), F32)],
        compiler_params=_params())(ba, arow, brow, dgb)


def _dot(a, b, prec=None):
    if prec is None:
        return jnp.dot(a.astype(BF16), b.astype(BF16), preferred_element_type=F32)
    return jnp.dot(a, b, precision=prec, preferred_element_type=F32)


def _dot_nt(a, b, prec=None):
    if prec is None:
        a, b = a.astype(BF16), b.astype(BF16)
    return lax.dot_general(a, b, (((1,), (1,)), ((), ())), precision=prec, preferred_element_type=F32)


def _dot_tn(a, b, prec=None):
    if prec is None:
        a, b = a.astype(BF16), b.astype(BF16)
    return lax.dot_general(a, b, (((0,), (0,)), ((), ())), precision=prec, preferred_element_type=F32)


def _gdr_chunk_terms(k, beta, g):
    c = GDR_CHUNK
    row = lax.broadcasted_iota(jnp.int32, (c, c), 0)
    col = lax.broadcasted_iota(jnp.int32, (c, c), 1)
    causal, strict = row >= col, row > col
    gcum = _dot(causal.astype(F32), jnp.broadcast_to(g, (c, c)), HIGHEST)
    diff = gcum - gcum.T
    decay = jnp.where(causal, jnp.exp(jnp.where(causal, diff, 0.0)), 0.0)
    kb = k * beta
    kk = _dot_nt(kb, k)
    return row, col, causal, strict, gcum, decay, kb, kk


def _unit_lower_inverse(a):
    c = a.shape[0]
    eye = (lax.broadcasted_iota(jnp.int32, (c, c), 0) == lax.broadcasted_iota(jnp.int32, (c, c), 1)).astype(F32)
    p = -a
    inv = eye + p
    step = 1
    while 2 * step < c:
        p = _dot(p, p, HIGH)
        inv = inv + _dot(inv, p, HIGH)
        step *= 2
    return inv


def _head_gates(gates, head):
    lane = lax.broadcasted_iota(jnp.int32, gates.shape, 1)
    beta = jnp.sum(jnp.where(lane == head, gates, 0.0), axis=1, keepdims=True)
    g = jnp.sum(jnp.where(lane == head + N_HEADS_A, gates, 0.0), axis=1, keepdims=True)
    return beta, g


def _gdr_fwd(q, k, v, gates, name, comm=None):
    t = q.shape[0]
    c = GDR_CHUNK
    n = t // c

    hps = GDR_HEADS_PER_STEP

    def one_head(hh, q_ref, k_ref, v_ref, gb_ref, o_ref, tm_ref, s_ref, state):
        cols = slice(hh * HEAD_A, (hh + 1) * HEAD_A)
        qv, kv, vv = q_ref[:, cols], k_ref[:, cols], v_ref[:, cols]
        beta, g = _head_gates(gb_ref[...], pl.program_id(0) * hps + hh)
        row, col, causal, strict, gcum, decay, kb, kk = _gdr_chunk_terms(kv, beta, g)
        tm = _unit_lower_inverse(jnp.where(strict, kk * decay, 0.0))
        e = jnp.exp(gcum)
        u = _dot(tm, vv * beta, HIGH)
        w = _dot(tm, kb * e, HIGH)
        p = jnp.where(causal, _dot_nt(qv, kv) * decay, 0.0)
        s = state[hh]
        s_ref[hh, 0] = s
        tm_ref[hh, 0] = tm
        vn = u - _dot(w, s)
        o_ref[:, cols] = _dot(qv * e, s) + _dot(p, vn)
        glast = gcum[c - 1:c, :]
        state[hh] = s * jnp.exp(glast) + _dot_tn(kv * jnp.exp(glast - gcum), vn)

    def body(*refs):
        state = refs[-1]

        @pl.when(pl.program_id(1) == 0)
        def _():
            state[...] = jnp.zeros_like(state)

        for hh in range(hps):
            one_head(hh, *refs)

    blk = pl.BlockSpec((c, hps * HEAD_A), lambda h, i: (i, h))
    mat = pl.BlockSpec((hps, 1, c, c), lambda h, i: (h, i, 0, 0))
    return _call(
        body, name=name, grid=(N_HEADS_A // hps, n),
        in_specs=[blk, blk, blk, pl.BlockSpec((c, LANE), lambda h, i: (i, 0))],
        out_specs=[blk, mat, mat],
        out_shape=[jax.ShapeDtypeStruct((t, WIDTH_A), F32), jax.ShapeDtypeStruct((N_HEADS_A, n, c, c), F32),
                   jax.ShapeDtypeStruct((N_HEADS_A, n, HEAD_A, HEAD_A), F32)],
        scratch_shapes=[pltpu.VMEM((hps, HEAD_A, HEAD_A), F32)], sem=("parallel", "arbitrary"),
        args=(q, k, v, gates), comm=comm)


def _gdr_bwd(q, k, v, gates, tm_all, s_all, do, name, comm=None):
    t = q.shape[0]
    c = GDR_CHUNK
    n = t // c

    hps = GDR_HEADS_PER_STEP

    def one_head(hh, q_ref, k_ref, v_ref, gb_ref, tm_ref, s_ref, do_ref, dq_ref, dk_ref, dv_ref, dgb_ref, dstate):
        cols = slice(hh * HEAD_A, (hh + 1) * HEAD_A)
        qv, kv, vv, dov = q_ref[:, cols], k_ref[:, cols], v_ref[:, cols], do_ref[:, cols]
        head = pl.program_id(0) * hps + hh
        beta, g = _head_gates(gb_ref[...], head)
        tm, s, dsp = tm_ref[hh, 0], s_ref[hh, 0], dstate[hh]
        row, col, causal, strict, gcum, decay, kb, kk = _gdr_chunk_terms(kv, beta, g)
        e = jnp.exp(gcum)
        vb, kbe = vv * beta, kb * e
        u = _dot(tm, vb, HIGH)
        w = _dot(tm, kbe, HIGH)
        qk = _dot_nt(qv, kv)
        p = jnp.where(causal, qk * decay, 0.0)
        vn = u - _dot(w, s)
        glast = gcum[c - 1:c, :]
        el = jnp.exp(glast)
        f = jnp.exp(glast - gcum)
        kd = kv * f
        qe = qv * e

        dvn = _dot_tn(p, dov) + _dot(kd, dsp)
        dglast = el[:, 0:1] * jnp.sum(s * dsp, keepdims=True)
        dkd = _dot_nt(vn, dsp)
        dk = dkd * f
        df = jnp.sum(dkd * kv, axis=1, keepdims=True) * f[:, 0:1]
        dglast = dglast + jnp.sum(df, keepdims=True)
        dgc = -df
        dp = jnp.where(causal, _dot_nt(dov, vn), 0.0)
        dqe = _dot_nt(dov, s)
        dq = dqe * e
        de = jnp.sum(dqe * qv, axis=1, keepdims=True)
        dstate[hh] = dsp * el + _dot_tn(qe, dov) - _dot_tn(w, dvn)
        dw = -_dot_nt(dvn, s)
        dvb = _dot_tn(tm, dvn, HIGH)
        dkbe = _dot_tn(tm, dw, HIGH)
        da = -jnp.where(strict, _dot_nt(dvb, u) + _dot_nt(dkbe, w), 0.0)
        dkk = da * decay
        dqk = dp * decay
        dd = da * kk + dp * qk
        dq = dq + _dot(dqk, kv)
        dk = dk + _dot_tn(dqk, qv)
        dkb = _dot(dkk, kv) + dkbe * e
        dk = dk + _dot_tn(dkk, kb)
        de = de + jnp.sum(dkbe * kb, axis=1, keepdims=True)
        dk = dk + dkb * beta
        dbeta = jnp.sum(dkb * kv, axis=1, keepdims=True) + jnp.sum(dvb * vv, axis=1, keepdims=True)
        m = dd * decay
        dgc = dgc + jnp.sum(m, axis=1, keepdims=True) - jnp.sum(m.T, axis=1, keepdims=True)
        dgc = dgc + de * e[:, 0:1]
        dgc = dgc + jnp.where(row[:, 0:1] == c - 1, dglast, 0.0)
        dg = _dot((row <= col).astype(F32), jnp.broadcast_to(dgc, (c, c)), HIGHEST)
        dq_ref[:, cols] = dq
        dk_ref[:, cols] = dk
        dv_ref[:, cols] = dvb * beta
        lane = lax.broadcasted_iota(jnp.int32, (c, LANE), 1)
        dgb_ref[hh] = jnp.where(lane == head, dbeta, jnp.where(lane == head + N_HEADS_A, dg, 0.0))

    def body(*refs):
        dstate = refs[-1]

        @pl.when(pl.program_id(1) == 0)
        def _():
            dstate[...] = jnp.zeros_like(dstate)

        for hh in range(hps):
            one_head(hh, *refs)

    blk = pl.BlockSpec((c, hps * HEAD_A), lambda h, i: (n - 1 - i, h))
    mat = pl.BlockSpec((hps, 1, c, c), lambda h, i: (h, n - 1 - i, 0, 0))
    return _call(
        body, name=name, grid=(N_HEADS_A // hps, n),
        in_specs=[blk, blk, blk, pl.BlockSpec((c, LANE), lambda h, i: (n - 1 - i, 0)), mat, mat, blk],
        out_specs=[blk, blk, blk, pl.BlockSpec((hps, c, LANE), lambda h, i: (h, n - 1 - i, 0))],
        out_shape=[jax.ShapeDtypeStruct((t, WIDTH_A), F32)] * 3 + [jax.ShapeDtypeStruct((N_HEADS_A, t, LANE), F32)],
        scratch_shapes=[pltpu.VMEM((hps, HEAD_A, HEAD_A), F32)], sem=("parallel", "arbitrary"),
        args=(q, k, v, gates, tm_all, s_all, do), comm=comm)


_B_NN, _B_NT, _B_TN = ((2,), (1,)), ((2,), (2,)), ((1,), (1,))


def _bdot(a, b, dims=_B_NN, prec=None):
    if prec is None:
        a, b = a.astype(BF16), b.astype(BF16)
    return lax.dot_general(a, b, (dims, ((0,), (0,))), precision=prec, preferred_element_type=F32)


def _heads_of(ref):
    return jnp.stack([ref[:, h * HEAD_A:(h + 1) * HEAD_A] for h in range(N_HEADS_A)])


def _all_head_gates(gates):
    pairs = [_head_gates(gates, h) for h in range(N_HEADS_A)]
    return jnp.stack([b for b, _ in pairs]), jnp.stack([g for _, g in pairs])


def _gdr_terms(k, beta, g):
    h, c = k.shape[0], GDR_CHUNK
    row = lax.broadcasted_iota(jnp.int32, (c, c), 0)
    col = lax.broadcasted_iota(jnp.int32, (c, c), 1)
    causal, strict = row >= col, row > col
    lower = jnp.broadcast_to(causal.astype(F32), (h, c, c))
    gcum = _bdot(lower, jnp.broadcast_to(g, (h, c, c)), prec=HIGHEST)
    diff = gcum - jnp.swapaxes(gcum, 1, 2)
    decay = jnp.where(causal, jnp.exp(jnp.where(causal, diff, 0.0)), 0.0)
    kb = k * beta
    return row, col, causal, strict, gcum, decay, kb, _bdot(kb, k, _B_NT)


def _unit_lower_inverses(a):
    c = a.shape[1]
    eye = (lax.broadcasted_iota(jnp.int32, (c, c), 0) == lax.broadcasted_iota(jnp.int32, (c, c), 1)).astype(F32)
    p = -a
    inv = eye + p
    step = 1
    while 2 * step < c:
        p = _bdot(p, p, prec=HIGH)
        inv = inv + _bdot(inv, p, prec=HIGH)
        step *= 2
    return inv


def _gdr_fwd(q, k, v, gates, name, comm=None):
    t = q.shape[0]
    c, nh = GDR_CHUNK, N_HEADS_A
    n = t // c

    def body(q_ref, k_ref, v_ref, gb_ref, o_ref, tm_ref, s_ref, state):
        @pl.when(pl.program_id(0) == 0)
        def _():
            state[...] = jnp.zeros_like(state)

        qv, kv, vv = _heads_of(q_ref), _heads_of(k_ref), _heads_of(v_ref)
        beta, g = _all_head_gates(gb_ref[...])
        row, col, causal, strict, gcum, decay, kb, kk = _gdr_terms(kv, beta, g)
        tm = _unit_lower_inverses(jnp.where(strict, kk * decay, 0.0))
        e = jnp.exp(gcum)
        u = _bdot(tm, vv * beta, prec=HIGH)
        w = _bdot(tm, kb * e, prec=HIGH)
        p = jnp.where(causal, _bdot(qv, kv, _B_NT) * decay, 0.0)
        s = state[...]
        s_ref[:, 0] = s
        tm_ref[:, 0] = tm
        vn = u - _bdot(w, s)
        o = _bdot(qv * e, s) + _bdot(p, vn)
        for h in range(nh):
            o_ref[:, h * HEAD_A:(h + 1) * HEAD_A] = o[h]
        glast = gcum[:, c - 1:c, :]
        state[...] = s * jnp.exp(glast) + _bdot(kv * jnp.exp(glast - gcum), vn, _B_TN)

    blk = pl.BlockSpec((c, WIDTH_A), lambda i: (i, 0))
    mat = pl.BlockSpec((nh, 1, c, c), lambda i: (0, i, 0, 0))
    return _call(
        body, name=name, grid=(n,), in_specs=[blk, blk, blk, pl.BlockSpec((c, LANE), lambda i: (i, 0))],
        out_specs=[blk, mat, mat],
        out_shape=[jax.ShapeDtypeStruct((t, WIDTH_A), F32), jax.ShapeDtypeStruct((nh, n, c, c), F32),
                   jax.ShapeDtypeStruct((nh, n, HEAD_A, HEAD_A), F32)],
        scratch_shapes=[pltpu.VMEM((nh, HEAD_A, HEAD_A), F32)], sem=("arbitrary",),
        args=(q, k, v, gates), comm=comm)


def _gdr_bwd(q, k, v, gates, tm_all, s_all, do, name, comm=None):
    t = q.shape[0]
    c, nh = GDR_CHUNK, N_HEADS_A
    n = t // c

    def body(q_ref, k_ref, v_ref, gb_ref, tm_ref, s_ref, do_ref, dq_ref, dk_ref, dv_ref, dgb_ref, dstate):
        @pl.when(pl.program_id(0) == 0)
        def _():
            dstate[...] = jnp.zeros_like(dstate)

        qv, kv, vv, dov = _heads_of(q_ref), _heads_of(k_ref), _heads_of(v_ref), _heads_of(do_ref)
        beta, g = _all_head_gates(gb_ref[...])
        tm, s, dsp = tm_ref[:, 0], s_ref[:, 0], dstate[...]
        row, col, causal, strict, gcum, decay, kb, kk = _gdr_terms(kv, beta, g)
        rowsum = lambda x: jnp.sum(x, axis=2, keepdims=True)
        e = jnp.exp(gcum)
        vb, kbe = vv * beta, kb * e
        u = _bdot(tm, vb, prec=HIGH)
        w = _bdot(tm, kbe, prec=HIGH)
        qk = _bdot(qv, kv, _B_NT)
        p = jnp.where(causal, qk * decay, 0.0)
        vn = u - _bdot(w, s)
        glast = gcum[:, c - 1:c, :]
        el = jnp.exp(glast)
        f = jnp.exp(glast - gcum)
        kd = kv * f
        qe = qv * e

        dvn = _bdot(p, dov, _B_TN) + _bdot(kd, dsp)
        dglast = el[:, :, 0:1] * jnp.sum(s * dsp, axis=(1, 2), keepdims=True)
        dkd = _bdot(vn, dsp, _B_NT)
        dk = dkd * f
        df = rowsum(dkd * kv) * f[:, :, 0:1]
        dglast = dglast + jnp.sum(df, axis=1, keepdims=True)
        dgc = -df
        dp = jnp.where(causal, _bdot(dov, vn, _B_NT), 0.0)
        dqe = _bdot(dov, s, _B_NT)
        dq = dqe * e
        de = rowsum(dqe * qv)
        dstate[...] = dsp * el + _bdot(qe, dov, _B_TN) - _bdot(w, dvn, _B_TN)
        dw = -_bdot(dvn, s, _B_NT)
        dvb = _bdot(tm, dvn, _B_TN, prec=HIGH)
        dkbe = _bdot(tm, dw, _B_TN, prec=HIGH)
        da = -jnp.where(strict, _bdot(dvb, u, _B_NT) + _bdot(dkbe, w, _B_NT), 0.0)
        dkk = da * decay
        dqk = dp * decay
        dd = da * kk + dp * qk
        dq = dq + _bdot(dqk, kv)
        dk = dk + _bdot(dqk, qv, _B_TN)
        dkb = _bdot(dkk, kv) + dkbe * e
        dk = dk + _bdot(dkk, kb, _B_TN)
        de = de + rowsum(dkbe * kb)
        dk = dk + dkb * beta
        dbeta = rowsum(dkb * kv) + rowsum(dvb * vv)
        m = dd * decay
        dgc = dgc + rowsum(m) - rowsum(jnp.swapaxes(m, 1, 2))
        dgc = dgc + de * e[:, :, 0:1]
        dgc = dgc + jnp.where(row[:, 0:1] == c - 1, dglast, 0.0)
        upper = jnp.broadcast_to((row <= col).astype(F32), (nh, c, c))
        dg = _bdot(upper, jnp.broadcast_to(dgc, (nh, c, c)), prec=HIGHEST)
        dv = dvb * beta
        for h in range(nh):
            cols = slice(h * HEAD_A, (h + 1) * HEAD_A)
            dq_ref[:, cols] = dq[h]
            dk_ref[:, cols] = dk[h]
            dv_ref[:, cols] = dv[h]
        head = lax.broadcasted_iota(jnp.int32, (nh, c, LANE), 0)
        lane = lax.broadcasted_iota(jnp.int32, (nh, c, LANE), 2)
        dgb_ref[...] = jnp.where(lane == head, dbeta, jnp.where(lane == head + nh, dg, 0.0))

    blk = pl.BlockSpec((c, WIDTH_A), lambda i: (n - 1 - i, 0))
    mat = pl.BlockSpec((nh, 1, c, c), lambda i: (0, n - 1 - i, 0, 0))
    return _call(
        body, name=name, grid=(n,),
        in_specs=[blk, blk, blk, pl.BlockSpec((c, LANE), lambda i: (n - 1 - i, 0)), mat, mat, blk],
        out_specs=[blk, blk, blk, pl.BlockSpec((nh, c, LANE), lambda i: (0, n - 1 - i, 0))],
        out_shape=[jax.ShapeDtypeStruct((t, WIDTH_A), F32)] * 3 + [jax.ShapeDtypeStruct((nh, t, LANE), F32)],
        scratch_shapes=[pltpu.VMEM((nh, HEAD_A, HEAD_A), F32)], sem=("arbitrary",),
        args=(q, k, v, gates, tm_all, s_all, do), comm=comm)


def _onorm_fwd(o, gate, g, name):
    t = o.shape[0]

    def body(o_ref, gate_ref, g_ref, y_ref):
        ov, gv = o_ref[...], gate_ref[...]
        r = lax.rsqrt(jnp.mean(ov * ov, axis=-1, keepdims=True) + RMS_EPS)
        y_ref[...] = (ov * r * g_ref[...] * gv * _sigmoid(gv)).astype(BF16)

    blk = pl.BlockSpec((t, HEAD_A), lambda j: (0, j))
    return pl.pallas_call(
        body, name=name, grid=(N_HEADS_A,), in_specs=[blk, blk, pl.BlockSpec((1, HEAD_A), lambda j: (0, 0))],
        out_specs=blk, out_shape=jax.ShapeDtypeStruct((t, WIDTH_A), BF16),
        compiler_params=_params(("parallel",)))(o, gate, g)


def _onorm_bwd(o, gate, g, dy, name):
    t = o.shape[0]

    def body(o_ref, gate_ref, g_ref, dy_ref, do_ref, dgate_ref, dg_ref):
        @pl.when(pl.program_id(0) == 0)
        def _():
            dg_ref[...] = jnp.zeros_like(dg_ref)

        ov, gv, dyv = o_ref[...], gate_ref[...], dy_ref[...].astype(F32)
        r = lax.rsqrt(jnp.mean(ov * ov, axis=-1, keepdims=True) + RMS_EPS)
        oh = ov * r
        sg, dsg = _silu_and_grad(gv)
        dgate_ref[...] = (dyv * oh * g_ref[...] * dsg).astype(BF16)
        dn = dyv * sg
        dg_ref[...] += jnp.sum(dn * oh, axis=0, keepdims=True)
        dng = dn * g_ref[...]
        do_ref[...] = r * (dng - oh * jnp.mean(dng * oh, axis=-1, keepdims=True))

    blk = pl.BlockSpec((t, HEAD_A), lambda j: (0, j))
    vec = pl.BlockSpec((1, HEAD_A), lambda j: (0, 0))
    return pl.pallas_call(
        body, name=name, grid=(N_HEADS_A,), in_specs=[blk, blk, vec, blk], out_specs=[blk, blk, vec],
        out_shape=[jax.ShapeDtypeStruct((t, WIDTH_A), F32), jax.ShapeDtypeStruct((t, WIDTH_A), BF16),
                   jax.ShapeDtypeStruct((1, HEAD_A), F32)],
        compiler_params=_params(("arbitrary",)))(o, gate, g, dy)


def _cmul(ar, ai, br, bi):
    return ar * br - ai * bi, ar * bi + ai * br


def _scan_tables(ar, ai, reverse):
    p1 = (ar, ai)
    p2 = _cmul(*p1, *p1)
    p4 = _cmul(*p2, *p2)
    p8 = _cmul(*p4, *p4)
    p3 = _cmul(*p2, *p1)
    p5 = _cmul(*p4, *p1)
    p6 = _cmul(*p4, *p2)
    p7 = _cmul(*p4, *p3)
    pows = [p1, p2, p3, p4, p5, p6, p7, p8]
    rows = lax.broadcasted_iota(jnp.int32, (8, ar.shape[1]), 0)
    tr = jnp.zeros((8, ar.shape[1]), F32)
    ti = jnp.zeros((8, ar.shape[1]), F32)
    for r in range(8):
        pw = pows[7 - r] if reverse else pows[r]
        tr = jnp.where(rows == r, pw[0], tr)
        ti = jnp.where(rows == r, pw[1], ti)
    return p1, p2, p4, p8, tr, ti


def _tile_scan(xr, xi, p1, p2, p4, reverse):
    rows = lax.broadcasted_iota(jnp.int32, xr.shape, 0)
    for s, (pr, pi) in ((1, p1), (2, p2), (4, p4)):
        if reverse:
            keep = rows < 8 - s
            sr, si = pltpu.roll(xr, 8 - s, 0), pltpu.roll(xi, 8 - s, 0)
        else:
            keep = rows >= s
            sr, si = pltpu.roll(xr, s, 0), pltpu.roll(xi, s, 0)
        sr, si = jnp.where(keep, sr, 0.0), jnp.where(keep, si, 0.0)
        mr, mi = _cmul(pr, pi, sr, si)
        xr, xi = xr + mr, xi + mi
    return xr, xi


def _s5_scan_fwd(bu, a, name, tb=512, comm=None):
    t = bu.shape[0]
    cb = SCAN_CB
    nt = t // tb

    def body(b_ref, a_ref, x_ref, carry):
        @pl.when(pl.program_id(1) == 0)
        def _():
            carry[...] = jnp.zeros_like(carry)

        ar, ai = a_ref[:, 0:cb], a_ref[:, cb:2 * cb]
        p1, p2, p4, p8, tr, ti = _scan_tables(ar, ai, False)

        def step(j, c):
            cr, ci = c
            i = pl.multiple_of(j * 8, 8)
            xr, xi = _tile_scan(b_ref[pl.ds(i, 8), 0:cb], b_ref[pl.ds(i, 8), cb:2 * cb], p1, p2, p4, False)
            mr, mi = _cmul(tr, ti, cr, ci)
            xr, xi = xr + mr, xi + mi
            x_ref[pl.ds(i, 8), 0:cb] = xr
            x_ref[pl.ds(i, 8), cb:2 * cb] = xi
            return xr[7:8, :], xi[7:8, :]

        cr, ci = lax.fori_loop(0, tb // 8, step, (carry[0:1, :], carry[1:2, :]), unroll=2)
        carry[0:1, :] = cr
        carry[1:2, :] = ci

    blk = pl.BlockSpec((tb, 2 * cb), lambda j, i: (i, j))
    return _call(
        body, name=name, grid=(SSM_CH // cb, nt),
        in_specs=[blk, pl.BlockSpec((1, 2 * cb), lambda j, i: (0, j))], out_specs=blk,
        out_shape=jax.ShapeDtypeStruct((t, 2 * SSM_CH), F32), scratch_shapes=[pltpu.VMEM((8, cb), F32)],
        sem=("parallel", "arbitrary"), args=(bu, a), comm=comm)


def _s5_scan_bwd(dx, x, a, name, tb=512, comm=None):
    t = dx.shape[0]
    cb = SCAN_CB
    nt = t // tb
    nj = tb // 8

    def body(d_ref, x_ref, xp_ref, a_ref, l_ref, da_ref, carry, acc):
        tblk = pl.program_id(1)

        @pl.when(tblk == 0)
        def _():
            carry[...] = jnp.zeros_like(carry)
            acc[...] = jnp.zeros_like(acc)

        ar, ai = a_ref[:, 0:cb], a_ref[:, cb:2 * cb]
        p1, p2, p4, p8, tr, ti = _scan_tables(ar, -ai, True)
        rows = lax.broadcasted_iota(jnp.int32, (8, cb), 0)

        def step(jj, c):
            cr, ci, sr_acc, si_acc = c
            j = nj - 1 - jj
            i = pl.multiple_of(j * 8, 8)
            lr, li = _tile_scan(d_ref[pl.ds(i, 8), 0:cb], d_ref[pl.ds(i, 8), cb:2 * cb], p1, p2, p4, True)
            mr, mi = _cmul(tr, ti, cr, ci)
            lr, li = lr + mr, li + mi
            l_ref[pl.ds(i, 8), 0:cb] = lr
            l_ref[pl.ds(i, 8), cb:2 * cb] = li
            ip = pl.multiple_of(jnp.maximum(j - 1, 0) * 8, 8)
            prev_r = jnp.where(j > 0, x_ref[pl.ds(ip, 8), 0:cb], xp_ref[:, 0:cb])
            prev_i = jnp.where(j > 0, x_ref[pl.ds(ip, 8), cb:2 * cb], xp_ref[:, cb:2 * cb])
            edge = jnp.where(jnp.logical_and(j == 0, tblk == nt - 1), 0.0, 1.0)
            xs_r = jnp.where(rows == 0, pltpu.roll(prev_r, 1, 0) * edge, pltpu.roll(x_ref[pl.ds(i, 8), 0:cb], 1, 0))
            xs_i = jnp.where(rows == 0, pltpu.roll(prev_i, 1, 0) * edge, pltpu.roll(x_ref[pl.ds(i, 8), cb:2 * cb], 1, 0))
            sr_acc = sr_acc + lr * xs_r + li * xs_i
            si_acc = si_acc + li * xs_r - lr * xs_i
            return lr[0:1, :], li[0:1, :], sr_acc, si_acc

        cr, ci, sr_acc, si_acc = lax.fori_loop(
            0, nj, step, (carry[0:1, :], carry[1:2, :], acc[:, 0:cb], acc[:, cb:2 * cb]))
        carry[0:1, :] = cr
        carry[1:2, :] = ci
        acc[:, 0:cb] = sr_acc
        acc[:, cb:2 * cb] = si_acc

        @pl.when(tblk == nt - 1)
        def _():
            da_ref[...] = jnp.sum(acc[...], axis=0, keepdims=True)

    blk = pl.BlockSpec((tb, 2 * cb), lambda j, i: (nt - 1 - i, j))
    prev = pl.BlockSpec((8, 2 * cb), lambda j, i: (jnp.maximum((nt - 1 - i) * (tb // 8) - 1, 0), j))
    vec = pl.BlockSpec((1, 2 * cb), lambda j, i: (0, j))
    return _call(
        body, name=name, grid=(SSM_CH // cb, nt), in_specs=[blk, blk, prev, vec], out_specs=[blk, vec],
        out_shape=[jax.ShapeDtypeStruct((t, 2 * SSM_CH), F32), jax.ShapeDtypeStruct((1, 2 * SSM_CH), F32)],
        scratch_shapes=[pltpu.VMEM((8, cb), F32), pltpu.VMEM((8, 2 * cb), F32)],
        sem=("parallel", "arbitrary"), args=(dx, x, x, a), comm=comm)


def _glu_fwd(yc, u, dvec, wg, bg, name, tr=256):
    t = yc.shape[0]

    def body(yc_ref, u_ref, d_ref, w_ref, b_ref, yl_ref, yb_ref):
        yl = yc_ref[...] + d_ref[...] * u_ref[...]
        yl_ref[...] = yl
        yg, _ = _gelu_and_grad(yl)
        z = jnp.dot(yg.astype(BF16), w_ref[...], preferred_element_type=F32) + b_ref[...]
        yb_ref[...] = (yg * _sigmoid(z)).astype(BF16)

    blk = pl.BlockSpec((tr, SSM_WIDTH), lambda i: (i, 0))
    vec = pl.BlockSpec((1, SSM_WIDTH), lambda i: (0, 0))
    return pl.pallas_call(
        body, name=name, grid=(t // tr,),
        in_specs=[blk, blk, vec, pl.BlockSpec((SSM_WIDTH, SSM_WIDTH), lambda i: (0, 0)), vec],
        out_specs=[blk, blk],
        out_shape=[jax.ShapeDtypeStruct((t, SSM_WIDTH), F32), jax.ShapeDtypeStruct((t, SSM_WIDTH), BF16)],
        compiler_params=_params(("parallel",)))(yc, u, dvec, wg, bg)


def _glu_bwd(yl, u, dvec, wg, bg, dyb, name, tr=256):
    t = yl.shape[0]

    def body(yl_ref, u_ref, d_ref, w_ref, b_ref, dy_ref, dyl_ref, du_ref, dw_ref, db_ref, dd_ref):
        @pl.when(pl.program_id(0) == 0)
        def _():
            dw_ref[...] = jnp.zeros_like(dw_ref)
            db_ref[...] = jnp.zeros_like(db_ref)
            dd_ref[...] = jnp.zeros_like(dd_ref)

        ylv, dyv, wv = yl_ref[...], dy_ref[...].astype(F32), w_ref[...]
        yg, dgelu = _gelu_and_grad(ylv)
        ygb = yg.astype(BF16)
        z = jnp.dot(ygb, wv, preferred_element_type=F32) + b_ref[...]
        sg = _sigmoid(z)
        dz = dyv * yg * sg * (1.0 - sg)
        dzb = dz.astype(BF16)
        dyg = dyv * sg + lax.dot_general(dzb, wv, (((1,), (1,)), ((), ())), preferred_element_type=F32)
        dyl = dyg * dgelu
        dyl_ref[...] = dyl.astype(BF16)
        du_ref[...] = dyl * d_ref[...]
        dw_ref[...] += lax.dot_general(ygb, dzb, (((0,), (0,)), ((), ())), preferred_element_type=F32)
        db_ref[...] += jnp.sum(dz, axis=0, keepdims=True)
        dd_ref[...] += jnp.sum(dyl * u_ref[...], axis=0, keepdims=True)

    blk = pl.BlockSpec((tr, SSM_WIDTH), lambda i: (i, 0))
    vec = pl.BlockSpec((1, SSM_WIDTH), lambda i: (0, 0))
    wsp = pl.BlockSpec((SSM_WIDTH, SSM_WIDTH), lambda i: (0, 0))
    return pl.pallas_call(
        body, name=name, grid=(t // tr,), in_specs=[blk, blk, vec, wsp, vec, blk],
        out_specs=[blk, blk, wsp, vec, vec],
        out_shape=[jax.ShapeDtypeStruct((t, SSM_WIDTH), BF16), jax.ShapeDtypeStruct((t, SSM_WIDTH), F32),
                   jax.ShapeDtypeStruct((SSM_WIDTH, SSM_WIDTH), F32), jax.ShapeDtypeStruct((1, SSM_WIDTH), F32),
                   jax.ShapeDtypeStruct((1, SSM_WIDTH), F32)],
        compiler_params=_params(("arbitrary",)))(yl, u, dvec, wg, bg, dyb)


def _mesh_pos():
    return lax.axis_index("x"), lax.axis_index("y"), lax.axis_index("c")


def _device_index():
    x, y, c = _mesh_pos()
    return 4 * x + 2 * y + c


def _gather_comm(arrays):
    na = len(arrays)

    def own_copy(ins, outs, sems, ai):
        return pltpu.make_async_copy(ins[ai], outs[ai].at[_device_index()], sems[2].at[ai])

    def ctx(ins, outs, sems):
        send_sems, recv_sems = sems[:2]
        x, y, c = _mesh_pos()
        chips = [(1 - x, y), (x, 1 - y), (1 - x, 1 - y)]

        def copy(ai, kk, block, to, own=False):
            slot = outs[ai].at[4 * block[0] + 2 * block[1] + block[2]]
            return pltpu.make_async_remote_copy(
                src_ref=ins[ai] if own else slot, dst_ref=slot, send_sem=send_sems.at[ai, kk],
                recv_sem=recv_sems.at[ai, kk], device_id=to, device_id_type=MESH)

        return (x, y, c), (x, y, 1 - c), chips, c, copy

    def start(ins, outs, sems):
        me, sibling, chips, c, copy = ctx(ins, outs, sems)
        for ai in range(na):
            copy(ai, 0, me, sibling, own=True).start()
            for j, chip in enumerate(chips):
                copy(ai, 1 + j, me, (*chip, c), own=True).start()
        for ai in range(na):
            own_copy(ins, outs, sems, ai).start()

    def mid(ins, outs, sems):
        me, sibling, chips, c, copy = ctx(ins, outs, sems)
        for ai in range(na):
            for j, chip in enumerate(chips):
                copy(ai, 1 + j, (*chip, c), me).wait_recv()
                copy(ai, 4 + j, (*chip, c), sibling).start()

    def end(ins, outs, sems):
        me, sibling, chips, c, copy = ctx(ins, outs, sems)
        for ai in range(na):
            copy(ai, 0, sibling, me).wait_recv()
            copy(ai, 0, me, sibling, own=True).wait_send()
            for j, chip in enumerate(chips):
                copy(ai, 4 + j, (*chip, 1 - c), me).wait_recv()
                copy(ai, 1 + j, me, (*chip, c), own=True).wait_send()
                copy(ai, 4 + j, (*chip, c), sibling).wait_send()
            own_copy(ins, outs, sems, ai).wait()

    return Comm(arrays, [jax.ShapeDtypeStruct((N_DEV,) + a.shape, a.dtype) for a in arrays],
                [pltpu.SemaphoreType.DMA((na, 7)), pltpu.SemaphoreType.DMA((na, 7)), pltpu.SemaphoreType.DMA((na,))],
                start, end, mid)


def _swap_comm(arrays):
    na = len(arrays)
    offs = np.concatenate([[0], np.cumsum([a.shape[1] for a in arrays])]).astype(int)

    def copies(ins, outs, sems):
        x, y, c = _mesh_pos()
        return [pltpu.make_async_remote_copy(
            src_ref=ins[ai].at[2 * k + 1 - c], dst_ref=outs[0].at[k, pl.ds(int(offs[ai]), arrays[ai].shape[1])],
            send_sem=sems[0].at[ai, k], recv_sem=sems[1].at[ai, k], device_id=(x, y, 1 - c), device_id_type=MESH)
            for ai in range(na) for k in range(4)]

    def start(ins, outs, sems):
        for cp in copies(ins, outs, sems):
            cp.start()

    def end(ins, outs, sems):
        for cp in copies(ins, outs, sems):
            cp.wait()

    return Comm(arrays, [jax.ShapeDtypeStruct((4, int(offs[-1]), PACK_COLS), arrays[0].dtype)],
                [pltpu.SemaphoreType.DMA((na, 4)), pltpu.SemaphoreType.DMA((na, 4))], start, end)


def _chips_comm(send):
    def copies(ins, outs, sems):
        x, y, c = _mesh_pos()
        chips = [(1 - x, y), (x, 1 - y), (1 - x, 1 - y)]
        return [pltpu.make_async_remote_copy(
            src_ref=ins[0].at[2 * cx + cy], dst_ref=outs[0].at[j], send_sem=sems[0].at[j], recv_sem=sems[1].at[j],
            device_id=(cx, cy, c), device_id_type=MESH) for j, (cx, cy) in enumerate(chips)]

    def start(ins, outs, sems):
        for cp in copies(ins, outs, sems):
            cp.start()

    def end(ins, outs, sems):
        for cp in copies(ins, outs, sems):
            cp.wait()

    return Comm([send], [jax.ShapeDtypeStruct((3,) + send.shape[1:], send.dtype)],
                [pltpu.SemaphoreType.DMA((3,)), pltpu.SemaphoreType.DMA((3,))], start, end)


def _all_gather(arrays, name):
    na = len(arrays)

    def body(*refs):
        ins, outs = refs[:na], refs[na:2 * na]
        send_sems, recv_sems, local_sems = refs[2 * na:]
        x, y, c = _mesh_pos()
        me, sibling = (x, y, c), (x, y, 1 - c)
        chips = [(1 - x, y), (x, 1 - y), (1 - x, 1 - y)]
        waits = []
        for ai in range(na):
            in_ref, out_ref = ins[ai], outs[ai]

            def slot(px, py, pc, out_ref=out_ref):
                return out_ref.at[4 * px + 2 * py + pc]

            def copy(kk, block, to, src=None, ai=ai, slot=slot):
                return pltpu.make_async_remote_copy(
                    src_ref=slot(*block) if src is None else src, dst_ref=slot(*block),
                    send_sem=send_sems.at[ai, kk], recv_sem=recv_sems.at[ai, kk], device_id=to, device_id_type=MESH)

            mine = pltpu.make_async_copy(in_ref, slot(*me), local_sems.at[ai])
            mine.start()
            first = [copy(0, me, sibling, src=in_ref)]
            first += [copy(1 + j, me, (*chip, c), src=in_ref) for j, chip in enumerate(chips)]
            for cp in first:
                cp.start()
            waits.append((copy, mine, first))
        sends = []
        for ai in range(na):
            copy, mine, first = waits[ai]
            passed = [copy(4 + j, (*chip, c), sibling) for j, chip in enumerate(chips)]
            for j, chip in enumerate(chips):
                copy(1 + j, (*chip, c), me).wait_recv()
                passed[j].start()
            sends.append(passed)
        for ai in range(na):
            copy, mine, first = waits[ai]
            copy(0, sibling, me).wait_recv()
            for j, chip in enumerate(chips):
                copy(4 + j, (*chip, 1 - c), me).wait_recv()
            for cp in first + sends[ai]:
                cp.wait_send()
            mine.wait()

    any_spec = pl.BlockSpec(memory_space=pl.ANY)
    return pl.pallas_call(
        body, name=name, in_specs=[any_spec] * na, out_specs=[any_spec] * na,
        out_shape=[jax.ShapeDtypeStruct((N_DEV,) + a.shape, a.dtype) for a in arrays],
        scratch_shapes=[pltpu.SemaphoreType.DMA((na, 7)), pltpu.SemaphoreType.DMA((na, 7)),
                        pltpu.SemaphoreType.DMA((na,))],
        compiler_params=pltpu.CompilerParams(has_side_effects=True))(*arrays)


def _swap_sibling(arrays, name):
    na = len(arrays)
    offs = np.concatenate([[0], np.cumsum([a.shape[1] for a in arrays])]).astype(int)
    rows = int(offs[-1])

    def body(*refs):
        ins, recv_ref = refs[:na], refs[na]
        send_sems, recv_sems = refs[na + 1:]
        x, y, c = _mesh_pos()
        started = []
        for ai in range(na):
            span = pl.ds(int(offs[ai]), arrays[ai].shape[1])
            for k in range(4):
                remote = pltpu.make_async_remote_copy(
                    src_ref=ins[ai].at[2 * k + 1 - c], dst_ref=recv_ref.at[k, span], send_sem=send_sems.at[ai, k],
                    recv_sem=recv_sems.at[ai, k], device_id=(x, y, 1 - c), device_id_type=MESH)
                remote.start()
                started.append(remote)
        for remote in started:
            remote.wait()

    any_spec = pl.BlockSpec(memory_space=pl.ANY)
    return pl.pallas_call(
        body, name=name, in_specs=[any_spec] * na, out_specs=any_spec,
        out_shape=jax.ShapeDtypeStruct((4, rows, PACK_COLS), arrays[0].dtype),
        scratch_shapes=[pltpu.SemaphoreType.DMA((na, 4)), pltpu.SemaphoreType.DMA((na, 4))])(*arrays)


def _exchange_chips(send, name):
    def body(s_ref, o_ref, send_sems, recv_sems):
        x, y, c = _mesh_pos()
        chips = [(1 - x, y), (x, 1 - y), (1 - x, 1 - y)]
        cps = [pltpu.make_async_remote_copy(
            src_ref=s_ref.at[2 * cx + cy], dst_ref=o_ref.at[j], send_sem=send_sems.at[j], recv_sem=recv_sems.at[j],
            device_id=(cx, cy, c), device_id_type=MESH) for j, (cx, cy) in enumerate(chips)]
        for cp in cps:
            cp.start()
        for cp in cps:
            cp.wait()

    any_spec = pl.BlockSpec(memory_space=pl.ANY)
    return pl.pallas_call(
        body, name=name, in_specs=[any_spec], out_specs=any_spec,
        out_shape=jax.ShapeDtypeStruct((3,) + send.shape[1:], send.dtype),
        scratch_shapes=[pltpu.SemaphoreType.DMA((3,)), pltpu.SemaphoreType.DMA((3,))])(send)


def _pair_sum(keep, recv, name, tr=464):
    nchip, rows, cols = keep.shape

    def body(g_ref, r_ref, o_ref):
        o_ref[...] = (g_ref[...].astype(F32) + r_ref[...].astype(F32)).astype(BF16)

    blk = pl.BlockSpec((1, tr, cols), lambda k, i: (k, i, 0))
    return pl.pallas_call(
        body, name=name, grid=(nchip, rows // tr), in_specs=[blk, blk], out_specs=blk,
        out_shape=jax.ShapeDtypeStruct((nchip, rows, cols), BF16),
        compiler_params=_params(("parallel", "parallel")))(keep, recv)


def _chip_sum(own, others, name, tr=464):
    _, rows, cols = own.shape
    chip = (2 * lax.axis_index("x") + lax.axis_index("y")).astype(jnp.int32).reshape(1)

    def body(chip_ref, own_ref, oth_ref, o_ref):
        del chip_ref
        acc = own_ref[0].astype(F32)
        for j in range(3):
            acc = acc + oth_ref[j].astype(F32)
        o_ref[...] = acc

    grid_spec = pltpu.PrefetchScalarGridSpec(
        num_scalar_prefetch=1, grid=(rows // tr,),
        in_specs=[pl.BlockSpec((1, tr, cols), lambda i, chip_ref: (chip_ref[0], i, 0)),
                  pl.BlockSpec((3, tr, cols), lambda i, chip_ref: (0, i, 0))],
        out_specs=pl.BlockSpec((tr, cols), lambda i, chip_ref: (i, 0)))
    return pl.pallas_call(
        body, name=name, grid_spec=grid_spec, out_shape=jax.ShapeDtypeStruct((rows, cols), F32),
        compiler_params=_params(("parallel",)))(chip, own, others)


def _sum_leading(parts, name, tr=464):
    nparts, rows, cols = parts.shape
    tr = tr if rows % tr == 0 else rows

    def body(p_ref, o_ref):
        acc = p_ref[0].astype(F32)
        for i in range(1, nparts):
            acc = acc + p_ref[i].astype(F32)
        o_ref[...] = acc

    return pl.pallas_call(
        body, name=name, grid=(rows // tr,),
        in_specs=[pl.BlockSpec((nparts, tr, cols), lambda i: (0, i, 0))],
        out_specs=pl.BlockSpec((tr, cols), lambda i: (i, 0)), out_shape=jax.ShapeDtypeStruct((rows, cols), F32),
        compiler_params=_params(("parallel",)))(parts)


def _adamw(w, g, m, v, name, comm=None):
    shape = w.shape
    cols = shape[-1]
    rows = int(np.prod(shape[:-1])) if len(shape) > 1 else 1
    w2, g2, m2, v2 = (a.reshape(rows, cols) for a in (w, g, m, v))
    tr = rows
    for cand in (512, 256, 128, 64, 32, 16, 8):
        if rows % cand == 0 and rows > cand:
            tr = cand
            break
    bc1, bc2 = 1.0 - ADAM_B1 ** ADAM_STEP, 1.0 - ADAM_B2 ** ADAM_STEP

    def body(w_ref, g_ref, m_ref, v_ref, d_ref, nm_ref, nv_ref):
        gv = g_ref[...]
        nm = ADAM_B1 * m_ref[...] + (1.0 - ADAM_B1) * gv
        nv = ADAM_B2 * v_ref[...] + (1.0 - ADAM_B2) * (gv * gv)
        nm_ref[...] = nm
        nv_ref[...] = nv
        d_ref[...] = -ADAM_LR * ((nm / bc1) / (jnp.sqrt(nv / bc2) + ADAM_EPS) + ADAM_WD * w_ref[...])

    blk = pl.BlockSpec((tr, cols), lambda i: (i, 0))
    res = _call(body, name=name, grid=(rows // tr,), in_specs=[blk] * 4, out_specs=[blk] * 3,
                out_shape=[jax.ShapeDtypeStruct((rows, cols), F32)] * 3, sem=("parallel",), args=(w2, g2, m2, v2),
                comm=comm)
    outs, couts = res if comm is not None else (res, None)
    outs = tuple(o.reshape(shape) for o in outs)
    return outs if comm is None else (outs, couts)


WEIGHT_NAMES = ['norm_mix_g', 'norm_xa_g', 'norm_ffn_g', 'norm_mem_g', 'norm_final_g', 'w_in_ab', 'conv_qkv_a',
                'a_log_a', 'dt_bias_a', 'onorm_g_a', 'ssm_lambda_re', 'ssm_lambda_im', 'ssm_b_re', 'ssm_b_im',
                'ssm_c_re', 'ssm_c_im', 'ssm_d', 'ssm_log_dt', 'w_glu_b', 'b_glu_b', 'w_out_ab', 'pool_w',
                'pool_scale', 'xa_wq', 'xa_wkv', 'xa_wo', 'ffn_w_up', 'ffn_conv', 'ffn_w_down']
BIG_SHARDED = {'w_in_ab': ((1, 1024, 2568), 2), 'w_glu_b': ((1, 512, 512), 1), 'w_out_ab': ((1, 1024, 1024), 1),
               'pool_w': ((1, 4, 256, 256), 2), 'xa_wq': ((2, 1024, 1024), 1), 'xa_wkv': ((2, 1024, 2048), 2),
               'xa_wo': ((2, 1024, 1024), 1), 'ffn_w_up': ((2, 1024, 5632), 2), 'ffn_w_down': ((2, 2816, 1024), 1)}
SMALL_SHARDED = {'conv_qkv_a': ((1, 4, 1536), 2), 'pool_scale': ((1, 1024), 1), 'ffn_conv': ((2, 3, 5632), 2)}
REPLICATED = {'norm_mix_g': (2, 1024), 'norm_xa_g': (2, 1024), 'norm_ffn_g': (2, 1024), 'norm_mem_g': (1024,),
              'norm_final_g': (1024,), 'a_log_a': (1, 4), 'dt_bias_a': (1, 4), 'onorm_g_a': (1, 128),
              'ssm_lambda_re': (1, 32, 64), 'ssm_lambda_im': (1, 32, 64), 'ssm_b_re': (1, 32, 64, 16),
              'ssm_b_im': (1, 32, 64, 16), 'ssm_c_re': (1, 32, 16, 64), 'ssm_c_im': (1, 32, 16, 64),
              'ssm_d': (1, 32, 16), 'ssm_log_dt': (1, 32), 'b_glu_b': (1, 512)}
PACK_ROW_ALIGN = 8


def _shard_shape(shape, axis):
    return tuple(s // N_DEV if i == axis else s for i, s in enumerate(shape))


def _round_up(n, m):
    return (n + m - 1) // m * m


def _pack(arrays):
    total = sum(int(np.prod(a.shape)) for a in arrays)
    padded = _round_up(total, PACK_COLS * PACK_ROW_ALIGN)
    parts = [a.astype(F32).reshape(-1) for a in arrays]
    if padded != total:
        parts.append(jnp.zeros((padded - total,), F32))
    return jnp.concatenate(parts).reshape(padded // PACK_COLS, PACK_COLS)


def _unpack(packed, shapes):
    flat, out, off = packed.reshape(-1), [], 0
    for shape in shapes:
        size = int(np.prod(shape))
        out.append(flat[off:off + size].reshape(shape))
        off += size
    return out


def _split_shards(full, axis):
    shape = full.shape
    s = shape[axis] // N_DEV
    a = full.reshape(shape[:axis] + (N_DEV, s) + shape[axis + 1:])
    return jnp.moveaxis(a, axis, 0).reshape(N_DEV, -1)


def _merge_shards(pieces, shape, axis):
    sh = _shard_shape(shape, axis)
    a = pieces.reshape((N_DEV,) + sh)
    a = jnp.moveaxis(a, 0, axis)
    return a.reshape(shape)


_SCAN_NB = SSM_CH // SCAN_CB


def _to_scan_layout(m, axis):
    shape = m.shape
    m = m.reshape(shape[:axis] + (2, _SCAN_NB, SCAN_CB) + shape[axis + 1:])
    return jnp.swapaxes(m, axis, axis + 1).reshape(shape)


def _from_scan_layout(m, axis):
    shape = m.shape
    m = m.reshape(shape[:axis] + (_SCAN_NB, 2, SCAN_CB) + shape[axis + 1:])
    return jnp.swapaxes(m, axis, axis + 1).reshape(shape)


def _s5_discretise(lam_re, lam_im, b_re, b_im, log_dt):
    dt = jnp.exp(log_dt)[:, None]
    mag = jnp.exp(lam_re * dt)
    ang = lam_im * dt
    lb_re, lb_im = mag * jnp.cos(ang), mag * jnp.sin(ang)
    den = lam_re * lam_re + lam_im * lam_im
    nr, ni = lb_re - 1.0, lb_im
    coef_re = (nr * lam_re + ni * lam_im) / den
    coef_im = (ni * lam_re - nr * lam_im) / den
    bb_re = coef_re[..., None] * b_re - coef_im[..., None] * b_im
    bb_im = coef_re[..., None] * b_im + coef_im[..., None] * b_re
    return lb_re, lb_im, bb_re, bb_im


_GROUPS_PER_BLOCK = N_GROUPS // _SCAN_NB
_U_BLOCK = _GROUPS_PER_BLOCK * SSM_GROUP


def _s5_matrices(lb_re, lb_im, bb_re, bb_im, c_re, c_im):
    eye = jnp.eye(_GROUPS_PER_BLOCK, dtype=F32)
    blocked = lambda m: m.reshape((_SCAN_NB, _GROUPS_PER_BLOCK) + m.shape[1:])
    bmat = lambda bb: jnp.einsum('jgph,gk->jghkp', blocked(bb), eye).reshape(_SCAN_NB, _U_BLOCK, SCAN_CB)
    cmat = lambda cc: jnp.einsum('jghp,gk->jkpgh', blocked(cc), eye).reshape(_SCAN_NB, SCAN_CB, _U_BLOCK)
    b_in = jnp.concatenate([bmat(bb_re), bmat(bb_im)], axis=2)
    c_out = jnp.concatenate([cmat(c_re), -cmat(c_im)], axis=1)
    a_row = _to_scan_layout(jnp.concatenate([lb_re.reshape(1, SSM_CH), lb_im.reshape(1, SSM_CH)], axis=1), 1)
    return b_in, c_out, a_row


def _s5_matrix_grads(db_in, dc_out, da_row):
    da_nat = _from_scan_layout(da_row, 1)
    eye = jnp.eye(_GROUPS_PER_BLOCK, dtype=F32)
    nb, gb = _SCAN_NB, _GROUPS_PER_BLOCK
    bgrad = lambda m: jnp.einsum('jghkp,gk->jgph', m.reshape(nb, gb, SSM_GROUP, gb, SSM_STATE), eye
                                 ).reshape(N_GROUPS, SSM_STATE, SSM_GROUP)
    cgrad = lambda m: jnp.einsum('jkpgh,gk->jghp', m.reshape(nb, gb, SSM_STATE, gb, SSM_GROUP), eye
                                 ).reshape(N_GROUPS, SSM_GROUP, SSM_STATE)
    dbb_re, dbb_im = bgrad(db_in[:, :, :SCAN_CB]), bgrad(db_in[:, :, SCAN_CB:])
    dc_re, dc_im = cgrad(dc_out[:, :SCAN_CB]), -cgrad(dc_out[:, SCAN_CB:])
    dlb_re = da_nat[0, :SSM_CH].reshape(N_GROUPS, SSM_STATE)
    dlb_im = da_nat[0, SSM_CH:].reshape(N_GROUPS, SSM_STATE)
    return dlb_re, dlb_im, dbb_re, dbb_im, dc_re, dc_im


def _as_pieces(a):
    return a.reshape(N_DEV, a.shape[0] // N_DEV, a.shape[1])


def _hybrid_fwd(xn, x, wts, p, weights, riders):
    sv = {}
    hq = _mm(xn, wts['w_qkv_t'], "nt", "l0_in_qkv")
    gate = _mm(xn, wts['w_gate_t'], "nt", "l0_in_gate")
    ba = _mm(xn, wts['w_ba_t'], "nt", "l0_in_ba")
    u = _mm(xn, wts['w_u_t'], "nt", "l0_in_u")
    conv = p['conv_qkv']
    q = _qkv_pre_fwd(hq, conv, 0, 4, True, HEAD_A ** -0.5, "l0_q_pre")
    k = _qkv_pre_fwd(hq, conv, 4, 4, True, 1.0, "l0_k_pre")
    v = _qkv_pre_fwd(hq, conv, 8, 4, False, 1.0, "l0_v_pre")
    gates = _gates_fwd(ba, p['arow'], p['brow'], "l0_gates")
    o, tm_all, s_all = riders.run("l0_gdr_fwd", _gdr_fwd, q, k, v, gates)
    wts['w_glu'], wts['w_out'] = weights.full['w_glu'], weights.full['w_out']
    y_a = _onorm_fwd(o, gate, p['onorm_g'], "l0_onorm")
    bu = riders.run("l0_s5_bu", _mm_bd, u, p['b_in'], "nn")
    xs = riders.run("l0_s5_scan", _s5_scan_fwd, bu, p['a_row'])
    yc = riders.run("l0_s5_cx", _mm_bd, xs, p['c_out'], "nn")
    yl, y_b = _glu_fwd(yc, u, p['d_row'], wts['w_glu'], p['b_glu'], "l0_glu")
    mixed = jnp.concatenate([y_a, y_b], axis=1)
    x1 = _mm(mixed, wts['w_out'], "nn", "l0_out", res=x)
    sv.update(hq=hq, gate=gate, ba=ba, u=u, q=q, k=k, v=v, gb=gates, o=o, tm=tm_all, s=s_all, xs=xs, yl=yl, mixed=mixed)
    return x1, sv


def _hybrid_bwd(dx1, xn, wts, p, sv, riders):
    gr = {}
    dmixed = _mm(dx1, wts['w_out'], "nt", "l0_out_dx", out_dtype=BF16)
    riders.grad('w_out', _as_pieces(_mm(sv['mixed'], dx1, "tn", "l0_out_dw", out_dtype=BF16)))
    dya, dyb = dmixed[:, :WIDTH_A], dmixed[:, WIDTH_A:]
    dyl, du_direct, dw_glu, gr['b_glu_b'], dd = _glu_bwd(
        sv['yl'], sv['u'], p['d_row'], wts['w_glu'], p['b_glu'], dyb, "l0_glu_bwd")
    riders.grad('w_glu', dw_glu.astype(BF16).reshape(N_DEV, -1, PACK_COLS))
    dxs = riders.run("l0_s5_cx_dx", _mm_bd, dyl, p['c_out'], "nt")
    dc_out = _mm_bd(sv['xs'], dyl, "tn", "l0_s5_cx_dw")
    lam, da_row = riders.run("l0_s5_scan_bwd", _s5_scan_bwd, dxs, sv['xs'], p['a_row'])
    du = _mm_bd(lam, p['b_in'], "nt", "l0_s5_bu_dx", res=du_direct, out_dtype=BF16)
    db_in = _mm_bd(sv['u'], lam, "tn", "l0_s5_bu_dw")
    gr['s5'] = (db_in, dc_out, da_row, dd)
    do, dgate, gr['onorm_g_a'] = _onorm_bwd(sv['o'], sv['gate'], p['onorm_g'], dya, "l0_onorm_bwd")
    dq, dk, dv, dgb = riders.run("l0_gdr_bwd", _gdr_bwd, sv['q'], sv['k'], sv['v'], sv['gb'], sv['tm'], sv['s'], do)
    conv = p['conv_qkv']
    dhq_q, dcw_q = _qkv_pre_bwd(sv['hq'], conv, dq, 0, 4, True, HEAD_A ** -0.5, "l0_q_pre_bwd")
    dhq_k, dcw_k = _qkv_pre_bwd(sv['hq'], conv, dk, 4, 4, True, 1.0, "l0_k_pre_bwd")
    dhq_v, dcw_v = _qkv_pre_bwd(sv['hq'], conv, dv, 8, 4, False, 1.0, "l0_v_pre_bwd")
    gr['conv_qkv_a'] = jnp.concatenate([dcw_q, dcw_k, dcw_v], axis=1)
    dhq = jnp.concatenate([dhq_q, dhq_k, dhq_v], axis=1)
    dba, da_log, ddt_bias = _gates_bwd(sv['ba'], p['arow'], p['brow'], dgb, "l0_gates_bwd")
    gr['a_log_a'], gr['dt_bias_a'] = da_log[:, 4:8], ddt_bias[:, 4:8]
    dw_qkv_t = _mm(dhq, xn, "tn", "l0_in_qkv_dw", out_dtype=BF16)
    dw_gate_t = _mm(dgate, xn, "tn", "l0_in_gate_dw", out_dtype=BF16)
    dw_ba_t = _mm(dba, xn, "tn", "l0_in_ba_dw", out_dtype=BF16)
    dw_u_t = _mm(du, xn, "tn", "l0_in_u_dw", out_dtype=BF16)
    dw_in_t = _as_pieces(jnp.concatenate([dw_qkv_t, dw_gate_t, dw_ba_t[:8], dw_u_t], axis=0))
    riders.grad('w_in_t', jnp.concatenate(
        [dw_in_t, jnp.zeros((N_DEV, dict(PIECES)['w_in_t'] - W_IN_PIECE, D_MODEL), BF16)], axis=1))
    dxn = riders.run("l0_in_qkv_dx", _mm, dhq, wts['w_qkv_t'], "nn")
    dxn = _mm(dgate, wts['w_gate_t'], "nn", "l0_in_gate_dx", res=dxn)
    dxn = _mm(dba, wts['w_ba_t'], "nn", "l0_in_ba_dx", res=dxn)
    dxn = _mm(du, wts['w_u_t'], "nn", "l0_in_u_dx", res=dxn)
    return dxn, gr


def _xa_fwd(x1, g, mem_n, wq, wkv_t, wo, tag, riders):
    xq = _rms_fwd(x1, g, BF16, tag + "_norm")
    q = _mm(xq, wq, "nn", tag + "_q", out_dtype=BF16)
    kv = _mm(mem_n, wkv_t, "nt", tag + "_kv", out_dtype=BF16)
    o = riders.run(tag + "_attn", _attn_fwd, q, kv)
    x2 = _mm(o, wo, "nn", tag + "_o", res=x1)
    return x2, dict(xq=xq, q=q, kv=kv, o=o)


def _xa_bwd(dx2, x1, g, mem_n, wq, wkv_t, wo, sv, tag, layer, riders):
    do = _mm(dx2, wo, "nt", tag + "_o_dx", out_dtype=BF16)
    riders.grad('wo%d' % layer, _as_pieces(_mm(sv['o'], dx2, "tn", tag + "_o_dw", out_dtype=BF16)))
    dq, dk, dv = _attn_bwd(sv['q'], sv['kv'], do, tag + "_attn_bwd")
    dkv = jnp.concatenate([dk, dv], axis=1).astype(BF16)
    dxq = _mm(dq, wq, "nt", tag + "_q_dx")
    riders.grad('wq%d' % layer, _as_pieces(_mm(sv['xq'], dq, "tn", tag + "_q_dw", out_dtype=BF16)))
    dmem_n = _mm(dkv, wkv_t, "nn", tag + "_kv_dx")
    riders.grad('wkv_t%d' % layer, _as_pieces(_mm(dkv, mem_n, "tn", tag + "_kv_dw", out_dtype=BF16)))
    dx1, dg = riders.run(tag + "_norm_bwd", _rms_bwd, x1, g, dxq, dx2)
    return dx1, dmem_n, dg


def _ffn_fwd(x2, g, w_up_t, conv, w_down, tag, riders):
    xf = _rms_fwd(x2, g, BF16, tag + "_norm")
    h = riders.run(tag + "_up", _mm, xf, w_up_t, "nt")
    a = riders.run(tag + "_act", _ffn_act_fwd, h, conv)
    x3 = _mm(a, w_down, "nn", tag + "_down", res=x2)
    return x3, dict(xf=xf, h=h, a=a)


def _ffn_bwd(dx3, x2, g, w_up_t, conv, w_down, sv, tag, layer, riders):
    da = _mm(dx3, w_down, "nt", tag + "_down_dx")
    riders.grad('down%d' % layer, _as_pieces(_mm(sv['a'], dx3, "tn", tag + "_down_dw", out_dtype=BF16)))
    dh, dconv = riders.run(tag + "_act_bwd", _ffn_act_bwd, sv['h'], conv, da)
    dxf = riders.run(tag + "_up_dx", _mm, dh, w_up_t, "nn")
    dw_up_t = riders.run(tag + "_up_dw", _mm, dh, sv['xf'], "tn", out_dtype=BF16)
    riders.grad('up_t%d' % layer, _as_pieces(dw_up_t))
    dx2, dg = riders.run(tag + "_norm_bwd", _rms_bwd, x2, g, dxf, dx3)
    return dx2, dconv, dg


BIG_NAMES, SMALL_NAMES, REP_NAMES = list(BIG_SHARDED), list(SMALL_SHARDED), list(REPLICATED)
BIG_SIZES = [int(np.prod(_shard_shape(*BIG_SHARDED[n]))) for n in BIG_NAMES]
SMALL_SIZES = [int(np.prod(_shard_shape(*SMALL_SHARDED[n]))) for n in SMALL_NAMES]


PIECES = [('w_in_t', 384), ('w_glu', 32), ('w_out', 128), ('pool_w', 32), ('wq0', 128), ('wq1', 128),
          ('wkv_t0', 256), ('wkv_t1', 256), ('wo0', 128), ('wo1', 128), ('up_t0', 704), ('up_t1', 704),
          ('down0', 352), ('down1', 352)]
PIECE_OFFS = dict(zip([k for k, _ in PIECES], np.concatenate([[0], np.cumsum([r for _, r in PIECES])[:-1]]).tolist()))
W_IN_ROWS = 4 * WIDTH_A + 2 * N_HEADS_A + SSM_WIDTH
W_IN_PIECE = W_IN_ROWS // N_DEV


def _row_tile(rows):
    return max(t for t in range(16, min(rows, 512) + 1, 16) if rows % t == 0)


class _Riders:
    def __init__(self):
        self.waiting = {}
        self.grads = {}
        self.groups = []
        self.reduced = {}

    def add(self, host, comm, then):
        self.waiting.setdefault(host, []).append((comm, then))

    def run(self, name, fn, *args, **kw):
        riders = self.waiting.pop(name, [])
        if not riders:
            return fn(*args, name=name, **kw)
        out, couts = fn(*args, name=name, comm=[c for c, _ in riders], **kw)
        for (_, then), got in zip(riders, couts):
            then(got)
        return out

    def exchange(self, comm, host, name, then):
        if host is None:
            then(_comm_only(comm, name))
        else:
            self.add(host, comm, then)

    def grad(self, key, pieces):
        self.grads[key] = pieces
        for group in [g for g in self.groups if all(k in self.grads for k in g[1])]:
            self.groups.remove(group)
            self._reduce(*group)

    def _reduce(self, name, keys, swap_host, chips_host):
        arrays = [self.grads[k] for k in keys]
        rows = sum(a.shape[1] for a in arrays)
        tile = _row_tile(rows)

        def after_chips(chip_sums, got):
            total = _chip_sum(chip_sums, got[0], name + "_chip_sum", tr=tile)
            off = 0
            for k, a in zip(keys, arrays):
                self.reduced[k] = total[off:off + a.shape[1]]
                off += a.shape[1]

        def after_swap(got):
            core = lax.axis_index("c")
            keep = jnp.concatenate(
                [lax.dynamic_index_in_dim(a.reshape(4, 2, a.shape[1], PACK_COLS), core, 1, keepdims=False)
                 for a in arrays], axis=1)
            chip_sums = _pair_sum(keep, got[0], name + "_pair_sum", tr=tile)
            self.exchange(_chips_comm(chip_sums), chips_host, name + "_to_chips",
                          functools.partial(after_chips, chip_sums))

        self.exchange(_swap_comm(arrays), swap_host, name + "_to_sibling", after_swap)


class _Weights:
    def __init__(self, inp):
        bf = lambda a: a.astype(BF16)
        local = {'w_in_t': bf(inp['w_in_ab'][0]).T, 'w_glu': bf(inp['w_glu_b'][0]), 'w_out': bf(inp['w_out_ab'][0]),
                 'pool_w': bf(inp['pool_w'][0]),
                 'small': _pack([inp[n] for n in SMALL_NAMES])}
        for l in range(2):
            local['wq%d' % l] = bf(inp['xa_wq'][l])
            local['wkv_t%d' % l] = bf(inp['xa_wkv'][l]).T
            local['wo%d' % l] = bf(inp['xa_wo'][l])
            local['up_t%d' % l] = bf(inp['ffn_w_up'][l]).T
            local['down%d' % l] = bf(inp['ffn_w_down'][l])
        self.local, self.full = local, {}

    def plan(self, keys):
        return _gather_comm([self.local[k] for k in keys])

    def land(self, keys, gathered):
        for k, g in zip(keys, gathered):
            if k == 'small':
                off = 0
                for n, size in zip(SMALL_NAMES, SMALL_SIZES):
                    self.full[n] = _merge_shards(g.reshape(N_DEV, -1)[:, off:off + size], *SMALL_SHARDED[n])
                    off += size
            elif k == 'pool_w':
                self.full[k] = jnp.swapaxes(g, 0, 1).reshape(len(POOL_WINDOWS), POOL_GROUP, POOL_GROUP)
            else:
                self.full[k] = g.reshape(N_DEV * g.shape[1], g.shape[2])


GATHER_FIRST = ['w_in_t', 'small']
GATHER_RIDES = [('l0_gdr_fwd', ['w_glu', 'w_out', 'wq0', 'wkv_t0', 'wo0', 'up_t0']),
                ('l0_s5_bu', ['pool_w', 'wq1']), ('l0_s5_scan', ['down0']), ('l0_s5_cx', ['wo1']),
                ('l0_xa_attn', ['wkv_t1']), ('l0_ffn_up', ['up_t1']), ('l0_ffn_act', ['down1'])]
GRAD_RIDES = [('g_down1', ['down1'], 'l1_ffn_act_bwd', 'l1_ffn_up_dx'),
              ('g_up1', ['up_t1'], 'l1_ffn_norm_bwd', 'l0_ffn_act_bwd'),
              ('g_xa1', ['wq1', 'wkv_t1', 'wo1', 'pool_w'], 'l1_mix_norm_bwd', 'l0_ffn_up_dx'),
              ('g_down0', ['down0'], 'l0_ffn_act_bwd', 'l0_ffn_up_dw'),
              ('g_up0', ['up_t0'], 'l0_ffn_norm_bwd', 'l0_gdr_bwd'),
              ('g_xa0', ['wq0', 'wkv_t0', 'wo0'], 'l0_xa_norm_bwd', 'l0_gdr_bwd'),
              ('g_out', ['w_out', 'w_glu'], 'l0_s5_cx_dx', 'l0_s5_scan_bwd'),
              ('g_in', ['w_in_t'], 'l0_in_qkv_dx', 'l0_mix_norm_bwd')]


def _local_step(inp):
    f32_of = lambda n: inp[n].astype(F32)
    weights = _Weights(inp)
    riders = _Riders()
    riders.groups = list(GRAD_RIDES)
    full = weights.full
    weights.land(GATHER_FIRST, _comm_only(weights.plan(GATHER_FIRST), "gather_first"))
    for host, keys in GATHER_RIDES:
        riders.add(host, weights.plan(keys), functools.partial(weights.land, keys))
    w_in_t = full['w_in_t']
    wts0 = dict(w_qkv_t=w_in_t[:3 * WIDTH_A], w_gate_t=w_in_t[3 * WIDTH_A:4 * WIDTH_A],
                w_ba_t=jnp.concatenate([w_in_t[4 * WIDTH_A:4 * WIDTH_A + 8], jnp.zeros((LANE - 8, D_MODEL), BF16)], 0),
                w_u_t=w_in_t[4 * WIDTH_A + 8:])
    lb_disc, disc_vjp = jax.vjp(_s5_discretise, f32_of('ssm_lambda_re')[0], f32_of('ssm_lambda_im')[0],
                                f32_of('ssm_b_re')[0], f32_of('ssm_b_im')[0], f32_of('ssm_log_dt')[0])
    b_in, c_out, a_row = _s5_matrices(*lb_disc, f32_of('ssm_c_re')[0], f32_of('ssm_c_im')[0])
    zeros4 = jnp.zeros((1, 4), F32)
    p0 = dict(conv_qkv=full['conv_qkv_a'][0], onorm_g=f32_of('onorm_g_a'),
              arow=jnp.concatenate([zeros4, f32_of('a_log_a'), jnp.zeros((1, LANE - 8), F32)], 1),
              brow=jnp.concatenate([zeros4, f32_of('dt_bias_a'), jnp.zeros((1, LANE - 8), F32)], 1),
              b_in=b_in.astype(BF16), c_out=c_out.astype(BF16), a_row=a_row,
              d_row=f32_of('ssm_d').reshape(1, SSM_WIDTH), b_glu=f32_of('b_glu_b'))

    x0 = inp['x'][0]
    mem_n = _rms_fwd(inp['mem'][0], inp['norm_mem_g'], BF16, "mem_norm")
    xn0 = _rms_fwd(x0, inp['norm_mix_g'][0], BF16, "l0_mix_norm")
    x1, sv_mix0 = _hybrid_fwd(xn0, x0, wts0, p0, weights, riders)
    x2, sv_xa0 = _xa_fwd(x1, inp['norm_xa_g'][0], mem_n, full['wq0'], full['wkv_t0'], full['wo0'], "l0_xa", riders)
    x3, sv_ffn0 = _ffn_fwd(x2, inp['norm_ffn_g'][0], full['up_t0'], full['ffn_conv'][0], full['down0'], "l0_ffn", riders)
    xn1 = _rms_fwd(x3, inp['norm_mix_g'][1], F32, "l1_mix_norm")
    x4 = _pool_fwd(xn1, full['pool_w'], full['pool_scale'], x3, "l1_pool")
    x5, sv_xa1 = _xa_fwd(x4, inp['norm_xa_g'][1], mem_n, full['wq1'], full['wkv_t1'], full['wo1'], "l1_xa", riders)
    x6, sv_ffn1 = _ffn_fwd(x5, inp['norm_ffn_g'][1], full['up_t1'], full['ffn_conv'][1], full['down1'], "l1_ffn", riders)
    loss_part, dx6, dg_final = _loss_head(x6, inp['norm_final_g'], inp['loss_target'][0], "loss_head")

    dx5, dconv1, dg_ffn1 = _ffn_bwd(dx6, x5, inp['norm_ffn_g'][1], full['up_t1'], full['ffn_conv'][1], full['down1'],
                                    sv_ffn1, "l1_ffn", 1, riders)
    dx4, dmem1, dg_xa1 = _xa_bwd(dx5, x4, inp['norm_xa_g'][1], mem_n, full['wq1'], full['wkv_t1'], full['wo1'],
                                 sv_xa1, "l1_xa", 1, riders)
    dxn1, dpool_w, dpool_scale = _pool_bwd(xn1, full['pool_w'], full['pool_scale'], dx4, "l1_pool_bwd")
    pool_pieces = jnp.swapaxes(dpool_w.astype(BF16).reshape(len(POOL_WINDOWS), N_DEV, -1, POOL_GROUP), 0, 1)
    riders.grad('pool_w', pool_pieces.reshape(N_DEV, -1, PACK_COLS))
    dx3, dg_mix1 = riders.run("l1_mix_norm_bwd", _rms_bwd, x3, inp['norm_mix_g'][1], dxn1, dx4)
    dx2, dconv0, dg_ffn0 = _ffn_bwd(dx3, x2, inp['norm_ffn_g'][0], full['up_t0'], full['ffn_conv'][0], full['down0'],
                                    sv_ffn0, "l0_ffn", 0, riders)
    dx1, dmem0, dg_xa0 = _xa_bwd(dx2, x1, inp['norm_xa_g'][0], mem_n, full['wq0'], full['wkv_t0'], full['wo0'],
                                 sv_xa0, "l0_xa", 0, riders)
    dxn0, g_mix0 = _hybrid_bwd(dx1, xn0, wts0, p0, sv_mix0, riders)
    grad_x, dg_mix0 = riders.run("l0_mix_norm_bwd", _rms_bwd, x0, inp['norm_mix_g'][0], dxn0, dx1)
    _, dg_mem = _rms_bwd(inp['mem'][0], inp['norm_mem_g'], dmem0 + dmem1, None, "mem_norm_bwd")
    assert not riders.waiting and not riders.groups, (list(riders.waiting), riders.groups)

    db_in, dc_out, da_row, dd = g_mix0['s5']
    dlb_re, dlb_im, dbb_re, dbb_im, dc_re, dc_im = _s5_matrix_grads(db_in, dc_out, da_row)
    dlam_re, dlam_im, dbr, dbi, dlog_dt = disc_vjp((dlb_re, dlb_im, dbb_re, dbb_im))

    rep_grads = {
        'norm_mix_g': jnp.concatenate([dg_mix0, dg_mix1], 0), 'norm_xa_g': jnp.concatenate([dg_xa0, dg_xa1], 0),
        'norm_ffn_g': jnp.concatenate([dg_ffn0, dg_ffn1], 0), 'norm_mem_g': dg_mem.reshape(-1),
        'norm_final_g': dg_final.reshape(-1), 'a_log_a': g_mix0['a_log_a'], 'dt_bias_a': g_mix0['dt_bias_a'],
        'onorm_g_a': g_mix0['onorm_g_a'], 'ssm_lambda_re': dlam_re[None], 'ssm_lambda_im': dlam_im[None],
        'ssm_b_re': dbr[None], 'ssm_b_im': dbi[None], 'ssm_c_re': dc_re[None], 'ssm_c_im': dc_im[None],
        'ssm_d': dd.reshape(1, N_GROUPS, SSM_GROUP), 'ssm_log_dt': dlog_dt[None], 'b_glu_b': g_mix0['b_glu_b']}
    small_grads = {'conv_qkv_a': g_mix0['conv_qkv_a'][None], 'pool_scale': dpool_scale,
                   'ffn_conv': jnp.stack([dconv0, dconv1])}
    for key, rows in PIECES:
        assert riders.reduced[key].shape == (rows, PACK_COLS), key
    return loss_part, grad_x, riders.reduced, rep_grads, small_grads


SMALL_GRADS_RIDE_ON = "adamw_ffn_w_up"


def _update(inp, loss_part, grad_x, big_reduced, rep_grads, small_grads):
    dev = _device_index()
    riders, gathered = _Riders(), {}
    misc_local = _pack([rep_grads[n] for n in REP_NAMES] + [small_grads[n] for n in SMALL_NAMES])
    riders.add(SMALL_GRADS_RIDE_ON, _gather_comm([misc_local]), lambda got: gathered.update(misc=got[0]))
    piece = lambda key, rows=None: big_reduced[key] if rows is None else big_reduced[key][:rows]
    both = lambda name, fn: jnp.stack([fn(piece(name + '0')), fn(piece(name + '1'))])
    ident, transpose = (lambda a: a), (lambda a: a.T)
    grads = {'w_in_ab': piece('w_in_t', W_IN_PIECE).T[None], 'w_glu_b': piece('w_glu').reshape(inp['w_glu_b'].shape),
             'w_out_ab': piece('w_out')[None], 'pool_w': piece('pool_w').reshape(inp['pool_w'].shape),
             'xa_wq': both('wq', ident), 'xa_wkv': both('wkv_t', transpose), 'xa_wo': both('wo', ident),
             'ffn_w_up': both('up_t', transpose), 'ffn_w_down': both('down', ident)}
    upd = {}
    for n in sorted(BIG_NAMES, key=lambda n: "adamw_" + n != SMALL_GRADS_RIDE_ON):
        upd[n] = riders.run("adamw_" + n, _adamw, inp[n], grads[n], inp['m_' + n], inp['v_' + n])
    misc_sum = _sum_leading(gathered['misc'], "small_grads_sum")
    misc = _unpack(misc_sum, [inp[n].shape for n in REP_NAMES] + [SMALL_SHARDED[n][0] for n in SMALL_NAMES])
    for n, g in zip(REP_NAMES, misc):
        grads[n] = g
    for n, g in zip(SMALL_NAMES, misc[len(REP_NAMES):]):
        grads[n] = lax.dynamic_index_in_dim(_split_shards(g, SMALL_SHARDED[n][1]), dev, 0, keepdims=False
                                            ).reshape(inp[n].shape)
    tiny_names = REP_NAMES + SMALL_NAMES
    rep_total = sum(int(np.prod(inp[n].shape)) for n in REP_NAMES)
    packs = [_pack([inp[prefix + n] for n in tiny_names]) for prefix in ('', 'm_', 'v_')]
    g_pack = _pack([misc_sum.reshape(-1)[:rep_total]] + [grads[n] for n in SMALL_NAMES])
    tiny_out = [_unpack(o, [inp[n].shape for n in tiny_names])
                for o in _adamw(packs[0], g_pack, packs[1], packs[2], "adamw_small")]
    for i, n in enumerate(tiny_names):
        upd[n] = tuple(o[i] for o in tiny_out)

    loss = lax.psum(loss_part[0, 0], ("x", "y", "c"))
    outs = [loss, grad_x[None]]
    outs += [grads[n] for n in WEIGHT_NAMES]
    for i in range(3):
        outs += [upd[n][i] for n in WEIGHT_NAMES]
    return tuple(outs)


def _step(inp):
    loss_part, grad_x, big_reduced, rep_grads, small_grads = _local_step(inp)
    return _update(inp, loss_part, grad_x, big_reduced, rep_grads, small_grads)


INPUT_NAMES = (['x', 'mem'] + WEIGHT_NAMES + ['loss_target'] + ['m_' + n for n in WEIGHT_NAMES]
               + ['v_' + n for n in WEIGHT_NAMES])


def kernel(x, mem, norm_mix_g, norm_xa_g, norm_ffn_g, norm_mem_g, norm_final_g, w_in_ab, conv_qkv_a, a_log_a, dt_bias_a, onorm_g_a, ssm_lambda_re, ssm_lambda_im, ssm_b_re, ssm_b_im, ssm_c_re, ssm_c_im, ssm_d, ssm_log_dt, w_glu_b, b_glu_b, w_out_ab, pool_w, pool_scale, xa_wq, xa_wkv, xa_wo, ffn_w_up, ffn_conv, ffn_w_down, loss_target, m_norm_mix_g, m_norm_xa_g, m_norm_ffn_g, m_norm_mem_g, m_norm_final_g, m_w_in_ab, m_conv_qkv_a, m_a_log_a, m_dt_bias_a, m_onorm_g_a, m_ssm_lambda_re, m_ssm_lambda_im, m_ssm_b_re, m_ssm_b_im, m_ssm_c_re, m_ssm_c_im, m_ssm_d, m_ssm_log_dt, m_w_glu_b, m_b_glu_b, m_w_out_ab, m_pool_w, m_pool_scale, m_xa_wq, m_xa_wkv, m_xa_wo, m_ffn_w_up, m_ffn_conv, m_ffn_w_down, v_norm_mix_g, v_norm_xa_g, v_norm_ffn_g, v_norm_mem_g, v_norm_final_g, v_w_in_ab, v_conv_qkv_a, v_a_log_a, v_dt_bias_a, v_onorm_g_a, v_ssm_lambda_re, v_ssm_lambda_im, v_ssm_b_re, v_ssm_b_im, v_ssm_c_re, v_ssm_c_im, v_ssm_d, v_ssm_log_dt, v_w_glu_b, v_b_glu_b, v_w_out_ab, v_pool_w, v_pool_scale, v_xa_wq, v_xa_wkv, v_xa_wo, v_ffn_w_up, v_ffn_conv, v_ffn_w_down):
    args = (x, mem, norm_mix_g, norm_xa_g, norm_ffn_g, norm_mem_g, norm_final_g, w_in_ab, conv_qkv_a, a_log_a, dt_bias_a, onorm_g_a, ssm_lambda_re, ssm_lambda_im, ssm_b_re, ssm_b_im, ssm_c_re, ssm_c_im, ssm_d, ssm_log_dt, w_glu_b, b_glu_b, w_out_ab, pool_w, pool_scale, xa_wq, xa_wkv, xa_wo, ffn_w_up, ffn_conv, ffn_w_down, loss_target, m_norm_mix_g, m_norm_xa_g, m_norm_ffn_g, m_norm_mem_g, m_norm_final_g, m_w_in_ab, m_conv_qkv_a, m_a_log_a, m_dt_bias_a, m_onorm_g_a, m_ssm_lambda_re, m_ssm_lambda_im, m_ssm_b_re, m_ssm_b_im, m_ssm_c_re, m_ssm_c_im, m_ssm_d, m_ssm_log_dt, m_w_glu_b, m_b_glu_b, m_w_out_ab, m_pool_w, m_pool_scale, m_xa_wq, m_xa_wkv, m_xa_wo, m_ffn_w_up, m_ffn_conv, m_ffn_w_down, v_norm_mix_g, v_norm_xa_g, v_norm_ffn_g, v_norm_mem_g, v_norm_final_g, v_w_in_ab, v_conv_qkv_a, v_a_log_a, v_dt_bias_a, v_onorm_g_a, v_ssm_lambda_re, v_ssm_lambda_im, v_ssm_b_re, v_ssm_b_im, v_ssm_c_re, v_ssm_c_im, v_ssm_d, v_ssm_log_dt, v_w_glu_b, v_b_glu_b, v_w_out_ab, v_pool_w, v_pool_scale, v_xa_wq, v_xa_wkv, v_xa_wo, v_ffn_w_up, v_ffn_conv, v_ffn_w_down)
    return _step(dict(zip(INPUT_NAMES, args)))
```

```python
import functools
import math

import numpy as np
import jax
import jax.numpy as jnp
from jax import lax
from jax.experimental import pallas as pl
from jax.experimental.pallas import tpu as pltpu

F32, BF16 = jnp.float32, jnp.bfloat16
HIGH, HIGHEST = lax.Precision.HIGH, lax.Precision.HIGHEST
MESH = pl.DeviceIdType.MESH

N_DEV = 8
SEQ, D_MODEL, MEM_LEN = 2048, 1024, 256
WIDTH_A, N_HEADS_A, HEAD_A, CONV_A = 512, 4, 128, 4
GDR_CHUNK = 128
GDR_HEADS_PER_STEP = 4
SSM_WIDTH, SSM_GROUP, N_GROUPS, SSM_STATE = 512, 16, 32, 64
SSM_CH = N_GROUPS * SSM_STATE
SCAN_CB = 512
POOL_WINDOWS = (2, 4, 8, 16)
POOL_GROUP = 256
N_HEADS_X, HEAD_X = 4, 256
D_FF, CONV_FFN = 2816, 3
RMS_EPS = 1e-6
ADAM_LR, ADAM_B1, ADAM_B2, ADAM_EPS, ADAM_WD, ADAM_STEP = 0.001, 0.9, 0.999, 1e-08, 0.01, 10
LANE = 128
PACK_COLS = 1024
VMEM_LIMIT_BYTES = 56 * 1024 * 1024


def _params(sem=None):
    return pltpu.CompilerParams(dimension_semantics=sem, vmem_limit_bytes=VMEM_LIMIT_BYTES)


class Comm:
    def __init__(self, inputs, out_shapes, sems, start, end, mid=None):
        self.inputs, self.out_shapes, self.sems = list(inputs), list(out_shapes), list(sems)
        self.start, self.mid, self.end = start, mid, end


def _merge_comms(comms):
    comms = [c for c in comms if c is not None]
    if not comms:
        return None, []
    bounds, ni, no, ns = [], 0, 0, 0
    for c in comms:
        bounds.append((ni, no, ns))
        ni, no, ns = ni + len(c.inputs), no + len(c.out_shapes), ns + len(c.sems)

    def phase(which):
        def run(ins, outs, sems):
            for c, (i0, o0, s0) in zip(comms, bounds):
                fn = getattr(c, which)
                if fn is not None:
                    fn(ins[i0:i0 + len(c.inputs)], outs[o0:o0 + len(c.out_shapes)], sems[s0:s0 + len(c.sems)])
        return run

    merged = Comm([a for c in comms for a in c.inputs], [s for c in comms for s in c.out_shapes],
                  [s for c in comms for s in c.sems], phase("start"), phase("end"), phase("mid"))
    return merged, [(o0, o0 + len(c.out_shapes)) for c, (_, o0, _) in zip(comms, bounds)]


def _call(body, *, name, grid, in_specs, out_specs, out_shape, args, scratch_shapes=(), sem=None, comm=None):
    single = not isinstance(out_shape, (list, tuple))
    out_specs_l = [out_specs] if single else list(out_specs)
    out_shape_l = [out_shape] if single else list(out_shape)
    scratch_shapes = list(scratch_shapes)
    merged, spans = _merge_comms(comm if isinstance(comm, (list, tuple)) else [comm])
    if merged is None:
        outs = pl.pallas_call(body, name=name, grid=grid, in_specs=list(in_specs), out_specs=out_specs_l,
                              out_shape=out_shape_l, scratch_shapes=scratch_shapes, compiler_params=_params(sem))(*args)
        outs = outs[0] if single else outs
        return outs if comm is None else (outs, [])
    n_in, n_out, n_scr = len(in_specs), len(out_specs_l), len(scratch_shapes)
    ci, co = len(merged.inputs), len(merged.out_shapes)
    total = int(np.prod(grid))

    def wrapped(*refs):
        ins, cins = refs[:n_in], refs[n_in:n_in + ci]
        outs, couts = refs[n_in + ci:n_in + ci + n_out], refs[n_in + ci + n_out:n_in + ci + n_out + co]
        scr, csems = refs[n_in + ci + n_out + co:n_in + ci + n_out + co + n_scr], refs[n_in + ci + n_out + co + n_scr:]
        lin = pl.program_id(0)
        for d in range(1, len(grid)):
            lin = lin * grid[d] + pl.program_id(d)
        pl.when(lin == 0)(lambda: merged.start(cins, couts, csems))
        body(*ins, *outs, *scr)
        mid_step = min((3 * total) // 4, total - 1)
        pl.when(lin == mid_step)(lambda: merged.mid(cins, couts, csems))
        pl.when(lin == total - 1)(lambda: merged.end(cins, couts, csems))

    any_spec = pl.BlockSpec(memory_space=pl.ANY)
    res = pl.pallas_call(
        wrapped, name=name, grid=grid, in_specs=list(in_specs) + [any_spec] * ci,
        out_specs=out_specs_l + [any_spec] * co, out_shape=out_shape_l + merged.out_shapes,
        scratch_shapes=scratch_shapes + merged.sems,
        compiler_params=_params(("arbitrary",) * len(grid)))(*args, *merged.inputs)
    outs, couts = res[:n_out], res[n_out:]
    return (outs[0] if single else list(outs)), [list(couts[a:b]) for a, b in spans]


def _comm_only(comm, name):
    def body():
        pass

    _, couts = _call(body, name=name, grid=(1,), in_specs=[], out_specs=[], out_shape=[], args=[], comm=comm)
    return couts[0]


def _tile(dim, pref):
    best = None
    for t in range(LANE, min(dim, pref) + 1, LANE):
        if dim % t == 0:
            best = t
    return best if best is not None else dim


MM_VMEM_BUDGET = 40 * 1024 * 1024


def _mm_tiles(m, n, k, a_bytes, b_bytes, o_bytes, r_bytes):
    for tk in (k, _tile(k, 2048), _tile(k, 1024), _tile(k, 512)):
        for tm, tn in ((1024, 1536), (1024, 1024), (1024, 512), (512, 512), (256, 512), (256, 256)):
            tm, tn = _tile(m, tm), _tile(n, tn)
            acc = 0 if tk == k else tm * tn * 4
            need = 2 * (tm * tk * a_bytes + tk * tn * b_bytes + tm * tn * (o_bytes + r_bytes)) + acc
            if need <= MM_VMEM_BUDGET:
                return tm, tn, tk
    raise ValueError("no matmul tiling fits VMEM")


def _mm(a, b, mode, name, out_dtype=F32, res=None, comm=None):
    if mode == "nn":
        (m, k), n = a.shape, b.shape[1]
    elif mode == "nt":
        (m, k), n = a.shape, b.shape[0]
    else:
        (k, m), n = a.shape, b.shape[1]
    tm, tn, tk = _mm_tiles(m, n, k, a.dtype.itemsize, b.dtype.itemsize, jnp.dtype(out_dtype).itemsize,
                           0 if res is None else res.dtype.itemsize)
    nk = k // tk
    dims = {"nn": ((1,), (0,)), "nt": ((1,), (1,)), "tn": ((0,), (0,))}[mode]

    def body(*refs):
        if res is None:
            a_ref, b_ref, o_ref = refs[:3]
            r_ref = None
        else:
            a_ref, b_ref, r_ref, o_ref = refs[:4]
        part = lax.dot_general(a_ref[...].astype(BF16), b_ref[...].astype(BF16), (dims, ((), ())),
                               preferred_element_type=F32)

        def finish(out):
            if r_ref is not None:
                out = out + r_ref[...].astype(F32)
            o_ref[...] = out.astype(out_dtype)

        if nk == 1:
            finish(part)
            return
        acc = refs[-1]
        kk = pl.program_id(2)

        @pl.when(kk == 0)
        def _():
            acc[...] = part

        @pl.when(kk > 0)
        def _():
            acc[...] += part

        @pl.when(kk == nk - 1)
        def _():
            finish(acc[...])

    a_spec = (pl.BlockSpec((tk, tm), lambda i, j, q: (q, i)) if mode == "tn"
              else pl.BlockSpec((tm, tk), lambda i, j, q: (i, q)))
    b_spec = (pl.BlockSpec((tn, tk), lambda i, j, q: (j, q)) if mode == "nt"
              else pl.BlockSpec((tk, tn), lambda i, j, q: (q, j)))
    o_spec = pl.BlockSpec((tm, tn), lambda i, j, q: (i, j))
    in_specs, args = [a_spec, b_spec], [a, b]
    if res is not None:
        in_specs.append(o_spec)
        args.append(res)
    return _call(body, name=name, grid=(m // tm, n // tn, nk), in_specs=in_specs, out_specs=o_spec,
                 out_shape=jax.ShapeDtypeStruct((m, n), out_dtype),
                 scratch_shapes=[] if nk == 1 else [pltpu.VMEM((tm, tn), F32)],
                 sem=("parallel", "parallel", "arbitrary"), args=args, comm=comm)


def _mm_bd(a, b, mode, name, out_dtype=F32, res=None, comm=None, tm=1024):
    if mode == "tn":
        k = a.shape[0]
        nb = min(a.shape[1], b.shape[1]) // LANE
        ma, n = a.shape[1] // nb, b.shape[1] // nb

        def body(a_ref, b_ref, o_ref):
            o_ref[0] = lax.dot_general(a_ref[...].astype(BF16), b_ref[...].astype(BF16), (((0,), (0,)), ((), ())),
                                       preferred_element_type=F32).astype(out_dtype)

        return _call(body, name=name, grid=(nb,),
                     in_specs=[pl.BlockSpec((k, ma), lambda j: (0, j)), pl.BlockSpec((k, n), lambda j: (0, j))],
                     out_specs=pl.BlockSpec((1, ma, n), lambda j: (j, 0, 0)),
                     out_shape=jax.ShapeDtypeStruct((nb, ma, n), out_dtype), sem=("parallel",), args=(a, b), comm=comm)
    m = a.shape[0]
    nb = b.shape[0]
    ka = a.shape[1] // nb
    n = b.shape[2] if mode == "nn" else b.shape[1]
    tm = _tile(m, tm)
    dims = ((1,), (0,)) if mode == "nn" else ((1,), (1,))

    def body(*refs):
        if res is None:
            a_ref, b_ref, o_ref = refs
            r_ref = None
        else:
            a_ref, b_ref, r_ref, o_ref = refs
        out = lax.dot_general(a_ref[...].astype(BF16), b_ref[0].astype(BF16), (dims, ((), ())),
                              preferred_element_type=F32)
        if r_ref is not None:
            out = out + r_ref[...].astype(F32)
        o_ref[...] = out.astype(out_dtype)

    o_spec = pl.BlockSpec((tm, n), lambda i, j: (i, j))
    in_specs = [pl.BlockSpec((tm, ka), lambda i, j: (i, j)), pl.BlockSpec((1,) + b.shape[1:], lambda i, j: (j, 0, 0))]
    args = [a, b]
    if res is not None:
        in_specs.append(o_spec)
        args.append(res)
    return _call(body, name=name, grid=(m // tm, nb), in_specs=in_specs, out_specs=o_spec,
                 out_shape=jax.ShapeDtypeStruct((m, nb * n), out_dtype), sem=("parallel", "parallel"),
                 args=args, comm=comm)


def _rms_fwd(x, g, out_dtype, name, tr=256):
    rows, d = x.shape

    def body(x_ref, g_ref, o_ref):
        xv = x_ref[...]
        r = lax.rsqrt(jnp.mean(xv * xv, axis=-1, keepdims=True) + RMS_EPS)
        o_ref[...] = (xv * r * g_ref[...]).astype(out_dtype)

    return pl.pallas_call(
        body, name=name, grid=(rows // tr,),
        in_specs=[pl.BlockSpec((tr, d), lambda i: (i, 0)), pl.BlockSpec((1, d), lambda i: (0, 0))],
        out_specs=pl.BlockSpec((tr, d), lambda i: (i, 0)), out_shape=jax.ShapeDtypeStruct((rows, d), out_dtype),
        compiler_params=_params(("parallel",)))(x, g.reshape(1, d))


def _rms_bwd(x, g, dy, dres, name, tr=256, comm=None):
    rows, d = x.shape

    def body(*refs):
        if dres is None:
            x_ref, g_ref, dy_ref, dx_ref, dg_ref = refs
            r_ref = None
        else:
            x_ref, g_ref, dy_ref, r_ref, dx_ref, dg_ref = refs

        @pl.when(pl.program_id(0) == 0)
        def _():
            dg_ref[...] = jnp.zeros_like(dg_ref)

        xv, dyv = x_ref[...], dy_ref[...].astype(F32)
        r = lax.rsqrt(jnp.mean(xv * xv, axis=-1, keepdims=True) + RMS_EPS)
        xh = xv * r
        dyg = dyv * g_ref[...]
        dx = r * (dyg - xh * jnp.mean(dyg * xh, axis=-1, keepdims=True))
        if r_ref is not None:
            dx = dx + r_ref[...]
        dx_ref[...] = dx
        dg_ref[...] += jnp.sum(dyv * xh, axis=0, keepdims=True)

    blk = pl.BlockSpec((tr, d), lambda i: (i, 0))
    vec = pl.BlockSpec((1, d), lambda i: (0, 0))
    in_specs, args = [blk, vec, blk], [x, g.reshape(1, d), dy]
    if dres is not None:
        in_specs.append(blk)
        args.append(dres)
    return _call(
        body, name=name, grid=(rows // tr,), in_specs=in_specs, out_specs=[blk, vec],
        out_shape=[jax.ShapeDtypeStruct((rows, d), F32), jax.ShapeDtypeStruct((1, d), F32)],
        sem=("arbitrary",), args=args, comm=comm)


def _loss_head(x, g, target, name, tr=256):
    rows, d = x.shape

    def body(x_ref, g_ref, t_ref, loss_ref, dx_ref, dg_ref):
        @pl.when(pl.program_id(0) == 0)
        def _():
            dg_ref[...] = jnp.zeros_like(dg_ref)
            loss_ref[...] = jnp.zeros_like(loss_ref)

        xv = x_ref[...]
        r = lax.rsqrt(jnp.mean(xv * xv, axis=-1, keepdims=True) + RMS_EPS)
        xh = xv * r
        err = xh * g_ref[...] - t_ref[...]
        loss_ref[...] += 0.5 * jnp.sum(jnp.mean(err * err, axis=-1, keepdims=True), keepdims=True)
        dyv = err * (1.0 / d)
        dyg = dyv * g_ref[...]
        dx_ref[...] = r * (dyg - xh * jnp.mean(dyg * xh, axis=-1, keepdims=True))
        dg_ref[...] += jnp.sum(dyv * xh, axis=0, keepdims=True)

    blk = pl.BlockSpec((tr, d), lambda i: (i, 0))
    vec = pl.BlockSpec((1, d), lambda i: (0, 0))
    return pl.pallas_call(
        body, name=name, grid=(rows // tr,), in_specs=[blk, vec, blk],
        out_specs=[pl.BlockSpec((1, 1), lambda i: (0, 0)), blk, vec],
        out_shape=[jax.ShapeDtypeStruct((1, 1), F32), jax.ShapeDtypeStruct((rows, d), F32),
                   jax.ShapeDtypeStruct((1, d), F32)],
        compiler_params=_params(("arbitrary",)))(x, g.reshape(1, d), target)


def _shift_down(x, s):
    rows = lax.broadcasted_iota(jnp.int32, x.shape, 0)
    return jnp.where(rows >= s, pltpu.roll(x, s, 0), 0.0)


def _shift_up(x, s):
    n = x.shape[0]
    rows = lax.broadcasted_iota(jnp.int32, x.shape, 0)
    return jnp.where(rows < n - s, pltpu.roll(x, n - s, 0), 0.0)


def _sigmoid(x):
    return 1.0 / (1.0 + jnp.exp(-x))


def _silu_and_grad(x):
    s = _sigmoid(x)
    return x * s, s * (1.0 + x * (1.0 - s))


_GELU_C0, _GELU_C1 = math.sqrt(2.0 / math.pi), 0.044715


def _gelu_and_grad(x):
    th = jnp.tanh(_GELU_C0 * (x + _GELU_C1 * x * x * x))
    y = 0.5 * x * (1.0 + th)
    dy = 0.5 * (1.0 + th) + 0.5 * x * (1.0 - th * th) * _GELU_C0 * (1.0 + 3.0 * _GELU_C1 * x * x)
    return y, dy


def _ffn_act_fwd(h, w, name, tc=256, comm=None):
    t = h.shape[0]
    nb = D_FF // tc

    def body(hg_ref, hv_ref, wg_ref, wv_ref, a_ref):
        def conv(x, wr):
            return wr[2:3, :] * x + wr[1:2, :] * _shift_down(x, 1) + wr[0:1, :] * _shift_down(x, 2)

        cg = conv(hg_ref[...], wg_ref[...])
        cv = conv(hv_ref[...], wv_ref[...])
        a_ref[...] = (cg * _sigmoid(cg) * cv).astype(BF16)

    return _call(
        body, name=name, grid=(nb,),
        in_specs=[pl.BlockSpec((t, tc), lambda j: (0, j)), pl.BlockSpec((t, tc), lambda j: (0, j + nb)),
                  pl.BlockSpec((CONV_FFN, tc), lambda j: (0, j)), pl.BlockSpec((CONV_FFN, tc), lambda j: (0, j + nb))],
        out_specs=pl.BlockSpec((t, tc), lambda j: (0, j)), out_shape=jax.ShapeDtypeStruct((t, D_FF), BF16),
        sem=("parallel",), args=(h, h, w, w), comm=comm)


def _ffn_act_bwd(h, w, da, name, tc=256, comm=None):
    t = h.shape[0]
    nb = D_FF // tc

    def body(hg_ref, hv_ref, wg_ref, wv_ref, da_ref, dhg_ref, dhv_ref, dwg_ref, dwv_ref):
        hg, hv, wg, wv = hg_ref[...], hv_ref[...], wg_ref[...], wv_ref[...]
        hg1, hg2, hv1, hv2 = _shift_down(hg, 1), _shift_down(hg, 2), _shift_down(hv, 1), _shift_down(hv, 2)
        cg = wg[2:3, :] * hg + wg[1:2, :] * hg1 + wg[0:1, :] * hg2
        cv = wv[2:3, :] * hv + wv[1:2, :] * hv1 + wv[0:1, :] * hv2
        sg, dsg = _silu_and_grad(cg)
        dav = da_ref[...].astype(F32)
        dcv = dav * sg
        dcg = dav * cv * dsg

        def conv_t(dc, wr):
            return wr[2:3, :] * dc + wr[1:2, :] * _shift_up(dc, 1) + wr[0:1, :] * _shift_up(dc, 2)

        dhg_ref[...] = conv_t(dcg, wg).astype(BF16)
        dhv_ref[...] = conv_t(dcv, wv).astype(BF16)
        dwg_ref[0:1, :] = jnp.sum(dcg * hg2, axis=0, keepdims=True)
        dwg_ref[1:2, :] = jnp.sum(dcg * hg1, axis=0, keepdims=True)
        dwg_ref[2:3, :] = jnp.sum(dcg * hg, axis=0, keepdims=True)
        dwv_ref[0:1, :] = jnp.sum(dcv * hv2, axis=0, keepdims=True)
        dwv_ref[1:2, :] = jnp.sum(dcv * hv1, axis=0, keepdims=True)
        dwv_ref[2:3, :] = jnp.sum(dcv * hv, axis=0, keepdims=True)

    big = lambda off: pl.BlockSpec((t, tc), lambda j: (0, j + off))
    small = lambda off: pl.BlockSpec((CONV_FFN, tc), lambda j: (0, j + off))
    res = _call(
        body, name=name, grid=(nb,),
        in_specs=[big(0), big(nb), small(0), small(nb), big(0)],
        out_specs=[big(0), big(0), small(0), small(0)],
        out_shape=[jax.ShapeDtypeStruct((t, D_FF), BF16), jax.ShapeDtypeStruct((t, D_FF), BF16),
                   jax.ShapeDtypeStruct((CONV_FFN, D_FF), F32), jax.ShapeDtypeStruct((CONV_FFN, D_FF), F32)],
        sem=("parallel",), args=(h, h, w, w, da), comm=comm)
    (dhg, dhv, dwg, dwv), couts = res if comm is not None else (res, None)
    out = (jnp.concatenate([dhg, dhv], axis=1), jnp.concatenate([dwg, dwv], axis=1))
    return out if comm is None else (out, couts)


def _attn_probs(q, k):
    s = lax.dot_general(q.astype(BF16), k.astype(BF16), (((1,), (1,)), ((), ())),
                        preferred_element_type=F32) * (HEAD_X ** -0.5)
    s = s - jnp.max(s, axis=-1, keepdims=True)
    p = jnp.exp(s)
    return p / jnp.sum(p, axis=-1, keepdims=True)


def _attn_fwd(q, kv, name, tq=512, comm=None):
    t = q.shape[0]

    def body(q_ref, k_ref, v_ref, o_ref):
        p = _attn_probs(q_ref[...], k_ref[...])
        o_ref[...] = jnp.dot(p.astype(BF16), v_ref[...].astype(BF16), preferred_element_type=F32).astype(BF16)

    return _call(
        body, name=name, grid=(N_HEADS_X, t // tq),
        in_specs=[pl.BlockSpec((tq, HEAD_X), lambda h, i: (i, h)),
                  pl.BlockSpec((MEM_LEN, HEAD_X), lambda h, i: (0, h)),
                  pl.BlockSpec((MEM_LEN, HEAD_X), lambda h, i: (0, h + N_HEADS_X))],
        out_specs=pl.BlockSpec((tq, HEAD_X), lambda h, i: (i, h)),
        out_shape=jax.ShapeDtypeStruct((t, N_HEADS_X * HEAD_X), BF16),
        sem=("parallel", "parallel"), args=(q, kv, kv), comm=comm)


def _attn_bwd(q, kv, do, name, tq=512):
    t = q.shape[0]

    def body(q_ref, k_ref, v_ref, do_ref, dq_ref, dk_ref, dv_ref):
        @pl.when(pl.program_id(1) == 0)
        def _():
            dk_ref[...] = jnp.zeros_like(dk_ref)
            dv_ref[...] = jnp.zeros_like(dv_ref)

        qb, kb, vb, dob = (r[...].astype(BF16) for r in (q_ref, k_ref, v_ref, do_ref))
        p = _attn_probs(qb, kb)
        dp = lax.dot_general(dob, vb, (((1,), (1,)), ((), ())), preferred_element_type=F32)
        ds = p * (dp - jnp.sum(dp * p, axis=-1, keepdims=True)) * (HEAD_X ** -0.5)
        dsb = ds.astype(BF16)
        dq_ref[...] = jnp.dot(dsb, kb, preferred_element_type=F32).astype(BF16)
        dk_ref[...] += lax.dot_general(dsb, qb, (((0,), (0,)), ((), ())), preferred_element_type=F32)
        dv_ref[...] += lax.dot_general(p.astype(BF16), dob, (((0,), (0,)), ((), ())), preferred_element_type=F32)

    qs = pl.BlockSpec((tq, HEAD_X), lambda h, i: (i, h))
    ms = pl.BlockSpec((MEM_LEN, HEAD_X), lambda h, i: (0, h))
    return pl.pallas_call(
        body, name=name, grid=(N_HEADS_X, t // tq),
        in_specs=[qs, ms, pl.BlockSpec((MEM_LEN, HEAD_X), lambda h, i: (0, h + N_HEADS_X)), qs],
        out_specs=[qs, ms, ms],
        out_shape=[jax.ShapeDtypeStruct((t, D_MODEL), BF16), jax.ShapeDtypeStruct((MEM_LEN, D_MODEL), F32),
                   jax.ShapeDtypeStruct((MEM_LEN, D_MODEL), F32)],
        compiler_params=_params(("parallel", "arbitrary")))(q, kv, kv, do)


def _pool_counts(t, win):
    pos = lax.broadcasted_iota(jnp.int32, (t, 1), 0).astype(F32) + 1.0
    return 1.0 / jnp.minimum(pos, float(win))


def _pool_delta(xv, win):
    s, step = xv, 1
    while step < win:
        s = s + _shift_down(s, step)
        step *= 2
    return s * _pool_counts(xv.shape[0], win) - xv


def _pool_delta_t(dv, win):
    s, step = dv * _pool_counts(dv.shape[0], win), 1
    while step < win:
        s = s + _shift_up(s, step)
        step *= 2
    return s - dv


def _pool_fwd(xn, w, scale, res, name):
    t = xn.shape[0]

    def make_branch(win, xn_ref, w_ref, s_ref, r_ref, o_ref):
        def branch():
            dl = _pool_delta(xn_ref[...], win)
            y = jnp.dot(dl.astype(BF16), w_ref[0], preferred_element_type=F32)
            o_ref[...] = r_ref[...] + y * s_ref[...]
        return branch

    def body(xn_ref, w_ref, s_ref, r_ref, o_ref):
        for gi, win in enumerate(POOL_WINDOWS):
            pl.when(pl.program_id(0) == gi)(make_branch(win, xn_ref, w_ref, s_ref, r_ref, o_ref))

    blk = pl.BlockSpec((t, POOL_GROUP), lambda g: (0, g))
    return pl.pallas_call(
        body, name=name, grid=(len(POOL_WINDOWS),),
        in_specs=[blk, pl.BlockSpec((1, POOL_GROUP, POOL_GROUP), lambda g: (g, 0, 0)),
                  pl.BlockSpec((1, POOL_GROUP), lambda g: (0, g)), blk],
        out_specs=blk, out_shape=jax.ShapeDtypeStruct((t, D_MODEL), F32),
        compiler_params=_params(("parallel",)))(xn, w, scale, res)


def _pool_bwd(xn, w, scale, dmix, name):
    t = xn.shape[0]

    def make_branch(win, xn_ref, w_ref, s_ref, d_ref, dxn_ref, dw_ref, ds_ref):
        def branch():
            dl = _pool_delta(xn_ref[...], win).astype(BF16)
            wv = w_ref[0]
            dm = d_ref[...]
            y = jnp.dot(dl, wv, preferred_element_type=F32)
            ds_ref[...] = jnp.sum(dm * y, axis=0, keepdims=True)
            dy = (dm * s_ref[...]).astype(BF16)
            dw_ref[0] = lax.dot_general(dl, dy, (((0,), (0,)), ((), ())), preferred_element_type=F32)
            ddl = lax.dot_general(dy, wv, (((1,), (1,)), ((), ())), preferred_element_type=F32)
            dxn_ref[...] = _pool_delta_t(ddl, win)
        return branch

    def body(*refs):
        for gi, win in enumerate(POOL_WINDOWS):
            pl.when(pl.program_id(0) == gi)(make_branch(win, *refs))

    blk = pl.BlockSpec((t, POOL_GROUP), lambda g: (0, g))
    wspec = pl.BlockSpec((1, POOL_GROUP, POOL_GROUP), lambda g: (g, 0, 0))
    vec = pl.BlockSpec((1, POOL_GROUP), lambda g: (0, g))
    return pl.pallas_call(
        body, name=name, grid=(len(POOL_WINDOWS),), in_specs=[blk, wspec, vec, blk], out_specs=[blk, wspec, vec],
        out_shape=[jax.ShapeDtypeStruct((t, D_MODEL), F32),
                   jax.ShapeDtypeStruct((len(POOL_WINDOWS), POOL_GROUP, POOL_GROUP), F32),
                   jax.ShapeDtypeStruct((1, D_MODEL), F32)],
        compiler_params=_params(("parallel",)))(xn, w, scale, dmix)


def _qkv_conv(h, wr):
    return (wr[3:4, :] * h + wr[2:3, :] * _shift_down(h, 1) + wr[1:2, :] * _shift_down(h, 2)
            + wr[0:1, :] * _shift_down(h, 3))


def _qkv_pre_fwd(h, w, col0, ncols, normalize, scale, name):
    t = h.shape[0]

    def body(h_ref, w_ref, o_ref):
        c = _qkv_conv(h_ref[...], w_ref[...])
        s = c * _sigmoid(c)
        if normalize:
            s = s * lax.rsqrt(jnp.sum(s * s, axis=-1, keepdims=True) + 1e-6) * scale
        o_ref[...] = s

    return pl.pallas_call(
        body, name=name, grid=(ncols,),
        in_specs=[pl.BlockSpec((t, HEAD_A), lambda j: (0, j + col0)), pl.BlockSpec((CONV_A, HEAD_A), lambda j: (0, j + col0))],
        out_specs=pl.BlockSpec((t, HEAD_A), lambda j: (0, j)), out_shape=jax.ShapeDtypeStruct((t, ncols * HEAD_A), F32),
        compiler_params=_params(("parallel",)))(h, w)


def _qkv_pre_bwd(h, w, dy, col0, ncols, normalize, scale, name):
    t = h.shape[0]

    def body(h_ref, w_ref, dy_ref, dh_ref, dw_ref):
        hv, wr, dyv = h_ref[...], w_ref[...], dy_ref[...]
        h1, h2, h3 = _shift_down(hv, 1), _shift_down(hv, 2), _shift_down(hv, 3)
        c = wr[3:4, :] * hv + wr[2:3, :] * h1 + wr[1:2, :] * h2 + wr[0:1, :] * h3
        s, dsilu = _silu_and_grad(c)
        if normalize:
            r = lax.rsqrt(jnp.sum(s * s, axis=-1, keepdims=True) + 1e-6)
            y = s * r
            dyv = dyv * scale
            ds = r * (dyv - y * jnp.sum(dyv * y, axis=-1, keepdims=True))
        else:
            ds = dyv
        dc = ds * dsilu
        dh = (wr[3:4, :] * dc + wr[2:3, :] * _shift_up(dc, 1) + wr[1:2, :] * _shift_up(dc, 2)
              + wr[0:1, :] * _shift_up(dc, 3))
        dh_ref[...] = dh.astype(BF16)
        dw_ref[0:1, :] = jnp.sum(dc * h3, axis=0, keepdims=True)
        dw_ref[1:2, :] = jnp.sum(dc * h2, axis=0, keepdims=True)
        dw_ref[2:3, :] = jnp.sum(dc * h1, axis=0, keepdims=True)
        dw_ref[3:4, :] = jnp.sum(dc * hv, axis=0, keepdims=True)

    return pl.pallas_call(
        body, name=name, grid=(ncols,),
        in_specs=[pl.BlockSpec((t, HEAD_A), lambda j: (0, j + col0)), pl.BlockSpec((CONV_A, HEAD_A), lambda j: (0, j + col0)),
                  pl.BlockSpec((t, HEAD_A), lambda j: (0, j))],
        out_specs=[pl.BlockSpec((t, HEAD_A), lambda j: (0, j)), pl.BlockSpec((CONV_A, HEAD_A), lambda j: (0, j))],
        out_shape=[jax.ShapeDtypeStruct((t, ncols * HEAD_A), BF16), jax.ShapeDtypeStruct((CONV_A, ncols * HEAD_A), F32)],
        compiler_params=_params(("parallel",)))(h, w, dy)


def _softplus(x):
    return jnp.maximum(x, 0.0) + jnp.log1p(jnp.exp(-jnp.abs(x)))


def _gates_fwd(ba, arow, brow, name):
    t = ba.shape[0]

    def body(x_ref, a_ref, b_ref, o_ref):
        xv = x_ref[...]
        lane = lax.broadcasted_iota(jnp.int32, xv.shape, 1)
        beta = _sigmoid(xv)
        g = -jnp.exp(a_ref[...]) * _softplus(xv + b_ref[...])
        o_ref[...] = jnp.where(lane < N_HEADS_A, beta, jnp.where(lane < 2 * N_HEADS_A, g, 0.0))

    return pl.pallas_call(body, name=name, out_shape=jax.ShapeDtypeStruct((t, LANE), F32),
                          compiler_params=_params())(ba, arow, brow)


def _gates_bwd(ba, arow, brow, dgb, name):
    t = ba.shape[0]

    def body(x_ref, a_ref, b_ref, d_ref, dx_ref, da_ref, db_ref):
        xv = x_ref[...]
        dv = d_ref[0] + d_ref[1] + d_ref[2] + d_ref[3]
        lane = lax.broadcasted_iota(jnp.int32, xv.shape, 1)
        beta = _sigmoid(xv)
        ea = jnp.exp(a_ref[...])
        z = xv + b_ref[...]
        dgv = jnp.where((lane >= N_HEADS_A) & (lane < 2 * N_HEADS_A), dv, 0.0) * (-ea)
        dz = dgv * _sigmoid(z)
        dx = jnp.where(lane < N_HEADS_A, dv * beta * (1.0 - beta), dz)
        dx_ref[...] = dx.astype(BF16)
        db_ref[...] = jnp.sum(dz, axis=0, keepdims=True)
        da_ref[...] = jnp.sum(dgv * _softplus(z), axis=0, keepdims=True)

    return pl.pallas_call(
        body, name=name,
        out_shape=[jax.ShapeDtypeStruct((t, LANE), BF16), jax.ShapeDtypeStruct((1, LANE), F32),
                   jax.ShapeDtypeStruct((1, LANE), F32)],
        compiler_params=_params())(ba, arow, brow, dgb)


def _dot(a, b, prec=None):
    if prec is None:
        return jnp.dot(a.astype(BF16), b.astype(BF16), preferred_element_type=F32)
    return jnp.dot(a, b, precision=prec, preferred_element_type=F32)


def _dot_nt(a, b, prec=None):
    if prec is None:
        a, b = a.astype(BF16), b.astype(BF16)
    return lax.dot_general(a, b, (((1,), (1,)), ((), ())), precision=prec, preferred_element_type=F32)


def _dot_tn(a, b, prec=None):
    if prec is None:
        a, b = a.astype(BF16), b.astype(BF16)
    return lax.dot_general(a, b, (((0,), (0,)), ((), ())), precision=prec, preferred_element_type=F32)


def _gdr_chunk_terms(k, beta, g):
    c = GDR_CHUNK
    row = lax.broadcasted_iota(jnp.int32, (c, c), 0)
    col = lax.broadcasted_iota(jnp.int32, (c, c), 1)
    causal, strict = row >= col, row > col
    gcum = _dot(causal.astype(F32), jnp.broadcast_to(g, (c, c)), HIGHEST)
    diff = gcum - gcum.T
    decay = jnp.where(causal, jnp.exp(jnp.where(causal, diff, 0.0)), 0.0)
    kb = k * beta
    kk = _dot_nt(kb, k)
    return row, col, causal, strict, gcum, decay, kb, kk


def _unit_lower_inverse(a):
    c = a.shape[0]
    eye = (lax.broadcasted_iota(jnp.int32, (c, c), 0) == lax.broadcasted_iota(jnp.int32, (c, c), 1)).astype(F32)
    p = -a
    inv = eye + p
    step = 1
    while 2 * step < c:
        p = _dot(p, p, HIGH)
        inv = inv + _dot(inv, p, HIGH)
        step *= 2
    return inv


def _head_gates(gates, head):
    lane = lax.broadcasted_iota(jnp.int32, gates.shape, 1)
    beta = jnp.sum(jnp.where(lane == head, gates, 0.0), axis=1, keepdims=True)
    g = jnp.sum(jnp.where(lane == head + N_HEADS_A, gates, 0.0), axis=1, keepdims=True)
    return beta, g


def _gdr_fwd(q, k, v, gates, name, comm=None):
    t = q.shape[0]
    c = GDR_CHUNK
    n = t // c

    hps = GDR_HEADS_PER_STEP

    def one_head(hh, q_ref, k_ref, v_ref, gb_ref, o_ref, tm_ref, s_ref, state):
        cols = slice(hh * HEAD_A, (hh + 1) * HEAD_A)
        qv, kv, vv = q_ref[:, cols], k_ref[:, cols], v_ref[:, cols]
        beta, g = _head_gates(gb_ref[...], pl.program_id(0) * hps + hh)
        row, col, causal, strict, gcum, decay, kb, kk = _gdr_chunk_terms(kv, beta, g)
        tm = _unit_lower_inverse(jnp.where(strict, kk * decay, 0.0))
        e = jnp.exp(gcum)
        u = _dot(tm, vv * beta, HIGH)
        w = _dot(tm, kb * e, HIGH)
        p = jnp.where(causal, _dot_nt(qv, kv) * decay, 0.0)
        s = state[hh]
        s_ref[hh, 0] = s
        tm_ref[hh, 0] = tm
        vn = u - _dot(w, s)
        o_ref[:, cols] = _dot(qv * e, s) + _dot(p, vn)
        glast = gcum[c - 1:c, :]
        state[hh] = s * jnp.exp(glast) + _dot_tn(kv * jnp.exp(glast - gcum), vn)

    def body(*refs):
        state = refs[-1]

        @pl.when(pl.program_id(1) == 0)
        def _():
            state[...] = jnp.zeros_like(state)

        for hh in range(hps):
            one_head(hh, *refs)

    blk = pl.BlockSpec((c, hps * HEAD_A), lambda h, i: (i, h))
    mat = pl.BlockSpec((hps, 1, c, c), lambda h, i: (h, i, 0, 0))
    return _call(
        body, name=name, grid=(N_HEADS_A // hps, n),
        in_specs=[blk, blk, blk, pl.BlockSpec((c, LANE), lambda h, i: (i, 0))],
        out_specs=[blk, mat, mat],
        out_shape=[jax.ShapeDtypeStruct((t, WIDTH_A), F32), jax.ShapeDtypeStruct((N_HEADS_A, n, c, c), F32),
                   jax.ShapeDtypeStruct((N_HEADS_A, n, HEAD_A, HEAD_A), F32)],
        scratch_shapes=[pltpu.VMEM((hps, HEAD_A, HEAD_A), F32)], sem=("parallel", "arbitrary"),
        args=(q, k, v, gates), comm=comm)


def _gdr_bwd(q, k, v, gates, tm_all, s_all, do, name, comm=None):
    t = q.shape[0]
    c = GDR_CHUNK
    n = t // c

    hps = GDR_HEADS_PER_STEP

    def one_head(hh, q_ref, k_ref, v_ref, gb_ref, tm_ref, s_ref, do_ref, dq_ref, dk_ref, dv_ref, dgb_ref, dstate):
        cols = slice(hh * HEAD_A, (hh + 1) * HEAD_A)
        qv, kv, vv, dov = q_ref[:, cols], k_ref[:, cols], v_ref[:, cols], do_ref[:, cols]
        head = pl.program_id(0) * hps + hh
        beta, g = _head_gates(gb_ref[...], head)
        tm, s, dsp = tm_ref[hh, 0], s_ref[hh, 0], dstate[hh]
        row, col, causal, strict, gcum, decay, kb, kk = _gdr_chunk_terms(kv, beta, g)
        e = jnp.exp(gcum)
        vb, kbe = vv * beta, kb * e
        u = _dot(tm, vb, HIGH)
        w = _dot(tm, kbe, HIGH)
        qk = _dot_nt(qv, kv)
        p = jnp.where(causal, qk * decay, 0.0)
        vn = u - _dot(w, s)
        glast = gcum[c - 1:c, :]
        el = jnp.exp(glast)
        f = jnp.exp(glast - gcum)
        kd = kv * f
        qe = qv * e

        dvn = _dot_tn(p, dov) + _dot(kd, dsp)
        dglast = el[:, 0:1] * jnp.sum(s * dsp, keepdims=True)
        dkd = _dot_nt(vn, dsp)
        dk = dkd * f
        df = jnp.sum(dkd * kv, axis=1, keepdims=True) * f[:, 0:1]
        dglast = dglast + jnp.sum(df, keepdims=True)
        dgc = -df
        dp = jnp.where(causal, _dot_nt(dov, vn), 0.0)
        dqe = _dot_nt(dov, s)
        dq = dqe * e
        de = jnp.sum(dqe * qv, axis=1, keepdims=True)
        dstate[hh] = dsp * el + _dot_tn(qe, dov) - _dot_tn(w, dvn)
        dw = -_dot_nt(dvn, s)
        dvb = _dot_tn(tm, dvn, HIGH)
        dkbe = _dot_tn(tm, dw, HIGH)
        da = -jnp.where(strict, _dot_nt(dvb, u) + _dot_nt(dkbe, w), 0.0)
        dkk = da * decay
        dqk = dp * decay
        dd = da * kk + dp * qk
        dq = dq + _dot(dqk, kv)
        dk = dk + _dot_tn(dqk, qv)
        dkb = _dot(dkk, kv) + dkbe * e
        dk = dk + _dot_tn(dkk, kb)
        de = de + jnp.sum(dkbe * kb, axis=1, keepdims=True)
        dk = dk + dkb * beta
        dbeta = jnp.sum(dkb * kv, axis=1, keepdims=True) + jnp.sum(dvb * vv, axis=1, keepdims=True)
        m = dd * decay
        dgc = dgc + jnp.sum(m, axis=1, keepdims=True) - jnp.sum(m.T, axis=1, keepdims=True)
        dgc = dgc + de * e[:, 0:1]
        dgc = dgc + jnp.where(row[:, 0:1] == c - 1, dglast, 0.0)
        dg = _dot((row <= col).astype(F32), jnp.broadcast_to(dgc, (c, c)), HIGHEST)
        dq_ref[:, cols] = dq
        dk_ref[:, cols] = dk
        dv_ref[:, cols] = dvb * beta
        lane = lax.broadcasted_iota(jnp.int32, (c, LANE), 1)
        dgb_ref[hh] = jnp.where(lane == head, dbeta, jnp.where(lane == head + N_HEADS_A, dg, 0.0))

    def body(*refs):
        dstate = refs[-1]

        @pl.when(pl.program_id(1) == 0)
        def _():
            dstate[...] = jnp.zeros_like(dstate)

        for hh in range(hps):
            one_head(hh, *refs)

    blk = pl.BlockSpec((c, hps * HEAD_A), lambda h, i: (n - 1 - i, h))
    mat = pl.BlockSpec((hps, 1, c, c), lambda h, i: (h, n - 1 - i, 0, 0))
    return _call(
        body, name=name, grid=(N_HEADS_A // hps, n),
        in_specs=[blk, blk, blk, pl.BlockSpec((c, LANE), lambda h, i: (n - 1 - i, 0)), mat, mat, blk],
        out_specs=[blk, blk, blk, pl.BlockSpec((hps, c, LANE), lambda h, i: (h, n - 1 - i, 0))],
        out_shape=[jax.ShapeDtypeStruct((t, WIDTH_A), F32)] * 3 + [jax.ShapeDtypeStruct((N_HEADS_A, t, LANE), F32)],
        scratch_shapes=[pltpu.VMEM((hps, HEAD_A, HEAD_A), F32)], sem=("parallel", "arbitrary"),
        args=(q, k, v, gates, tm_all, s_all, do), comm=comm)


_B_NN, _B_NT, _B_TN = ((2,), (1,)), ((2,), (2,)), ((1,), (1,))


def _bdot(a, b, dims=_B_NN, prec=None):
    if prec is None:
        a, b = a.astype(BF16), b.astype(BF16)
    return lax.dot_general(a, b, (dims, ((0,), (0,))), precision=prec, preferred_element_type=F32)


def _heads_of(ref):
    return jnp.stack([ref[:, h * HEAD_A:(h + 1) * HEAD_A] for h in range(N_HEADS_A)])


def _all_head_gates(gates):
    pairs = [_head_gates(gates, h) for h in range(N_HEADS_A)]
    return jnp.stack([b for b, _ in pairs]), jnp.stack([g for _, g in pairs])


def _gdr_terms(k, beta, g):
    h, c = k.shape[0], GDR_CHUNK
    row = lax.broadcasted_iota(jnp.int32, (c, c), 0)
    col = lax.broadcasted_iota(jnp.int32, (c, c), 1)
    causal, strict = row >= col, row > col
    lower = jnp.broadcast_to(causal.astype(F32), (h, c, c))
    gcum = _bdot(lower, jnp.broadcast_to(g, (h, c, c)), prec=HIGHEST)
    diff = gcum - jnp.swapaxes(gcum, 1, 2)
    decay = jnp.where(causal, jnp.exp(jnp.where(causal, diff, 0.0)), 0.0)
    kb = k * beta
    return row, col, causal, strict, gcum, decay, kb, _bdot(kb, k, _B_NT)


def _unit_lower_inverses(a):
    c = a.shape[1]
    eye = (lax.broadcasted_iota(jnp.int32, (c, c), 0) == lax.broadcasted_iota(jnp.int32, (c, c), 1)).astype(F32)
    p = -a
    inv = eye + p
    step = 1
    while 2 * step < c:
        p = _bdot(p, p, prec=HIGH)
        inv = inv + _bdot(inv, p, prec=HIGH)
        step *= 2
    return inv


def _gdr_fwd(q, k, v, gates, name, comm=None):
    t = q.shape[0]
    c, nh = GDR_CHUNK, N_HEADS_A
    n = t // c

    def body(q_ref, k_ref, v_ref, gb_ref, o_ref, tm_ref, s_ref, state):
        @pl.when(pl.program_id(0) == 0)
        def _():
            state[...] = jnp.zeros_like(state)

        qv, kv, vv = _heads_of(q_ref), _heads_of(k_ref), _heads_of(v_ref)
        beta, g = _all_head_gates(gb_ref[...])
        row, col, causal, strict, gcum, decay, kb, kk = _gdr_terms(kv, beta, g)
        tm = _unit_lower_inverses(jnp.where(strict, kk * decay, 0.0))
        e = jnp.exp(gcum)
        u = _bdot(tm, vv * beta, prec=HIGH)
        w = _bdot(tm, kb * e, prec=HIGH)
        p = jnp.where(causal, _bdot(qv, kv, _B_NT) * decay, 0.0)
        s = state[...]
        s_ref[:, 0] = s
        tm_ref[:, 0] = tm
        vn = u - _bdot(w, s)
        o = _bdot(qv * e, s) + _bdot(p, vn)
        for h in range(nh):
            o_ref[:, h * HEAD_A:(h + 1) * HEAD_A] = o[h]
        glast = gcum[:, c - 1:c, :]
        state[...] = s * jnp.exp(glast) + _bdot(kv * jnp.exp(glast - gcum), vn, _B_TN)

    blk = pl.BlockSpec((c, WIDTH_A), lambda i: (i, 0))
    mat = pl.BlockSpec((nh, 1, c, c), lambda i: (0, i, 0, 0))
    return _call(
        body, name=name, grid=(n,), in_specs=[blk, blk, blk, pl.BlockSpec((c, LANE), lambda i: (i, 0))],
        out_specs=[blk, mat, mat],
        out_shape=[jax.ShapeDtypeStruct((t, WIDTH_A), F32), jax.ShapeDtypeStruct((nh, n, c, c), F32),
                   jax.ShapeDtypeStruct((nh, n, HEAD_A, HEAD_A), F32)],
        scratch_shapes=[pltpu.VMEM((nh, HEAD_A, HEAD_A), F32)], sem=("arbitrary",),
        args=(q, k, v, gates), comm=comm)


def _gdr_bwd(q, k, v, gates, tm_all, s_all, do, name, comm=None):
    t = q.shape[0]
    c, nh = GDR_CHUNK, N_HEADS_A
    n = t // c

    def body(q_ref, k_ref, v_ref, gb_ref, tm_ref, s_ref, do_ref, dq_ref, dk_ref, dv_ref, dgb_ref, dstate):
        @pl.when(pl.program_id(0) == 0)
        def _():
            dstate[...] = jnp.zeros_like(dstate)

        qv, kv, vv, dov = _heads_of(q_ref), _heads_of(k_ref), _heads_of(v_ref), _heads_of(do_ref)
        beta, g = _all_head_gates(gb_ref[...])
        tm, s, dsp = tm_ref[:, 0], s_ref[:, 0], dstate[...]
        row, col, causal, strict, gcum, decay, kb, kk = _gdr_terms(kv, beta, g)
        rowsum = lambda x: jnp.sum(x, axis=2, keepdims=True)
        e = jnp.exp(gcum)
        vb, kbe = vv * beta, kb * e
        u = _bdot(tm, vb, prec=HIGH)
        w = _bdot(tm, kbe, prec=HIGH)
        qk = _bdot(qv, kv, _B_NT)
        p = jnp.where(causal, qk * decay, 0.0)
        vn = u - _bdot(w, s)
        glast = gcum[:, c - 1:c, :]
        el = jnp.exp(glast)
        f = jnp.exp(glast - gcum)
        kd = kv * f
        qe = qv * e

        dvn = _bdot(p, dov, _B_TN) + _bdot(kd, dsp)
        dglast = el[:, :, 0:1] * jnp.sum(s * dsp, axis=(1, 2), keepdims=True)
        dkd = _bdot(vn, dsp, _B_NT)
        dk = dkd * f
        df = rowsum(dkd * kv) * f[:, :, 0:1]
        dglast = dglast + jnp.sum(df, axis=1, keepdims=True)
        dgc = -df
        dp = jnp.where(causal, _bdot(dov, vn, _B_NT), 0.0)
        dqe = _bdot(dov, s, _B_NT)
        dq = dqe * e
        de = rowsum(dqe * qv)
        dstate[...] = dsp * el + _bdot(qe, dov, _B_TN) - _bdot(w, dvn, _B_TN)
        dw = -_bdot(dvn, s, _B_NT)
        dvb = _bdot(tm, dvn, _B_TN, prec=HIGH)
        dkbe = _bdot(tm, dw, _B_TN, prec=HIGH)
        da = -jnp.where(strict, _bdot(dvb, u, _B_NT) + _bdot(dkbe, w, _B_NT), 0.0)
        dkk = da * decay
        dqk = dp * decay
        dd = da * kk + dp * qk
        dq = dq + _bdot(dqk, kv)
        dk = dk + _bdot(dqk, qv, _B_TN)
        dkb = _bdot(dkk, kv) + dkbe * e
        dk = dk + _bdot(dkk, kb, _B_TN)
        de = de + rowsum(dkbe * kb)
        dk = dk + dkb * beta
        dbeta = rowsum(dkb * kv) + rowsum(dvb * vv)
        m = dd * decay
        dgc = dgc + rowsum(m) - rowsum(jnp.swapaxes(m, 1, 2))
        dgc = dgc + de * e[:, :, 0:1]
        dgc = dgc + jnp.where(row[:, 0:1] == c - 1, dglast, 0.0)
        upper = jnp.broadcast_to((row <= col).astype(F32), (nh, c, c))
        dg = _bdot(upper, jnp.broadcast_to(dgc, (nh, c, c)), prec=HIGHEST)
        dv = dvb * beta
        for h in range(nh):
            cols = slice(h * HEAD_A, (h + 1) * HEAD_A)
            dq_ref[:, cols] = dq[h]
            dk_ref[:, cols] = dk[h]
            dv_ref[:, cols] = dv[h]
        head = lax.broadcasted_iota(jnp.int32, (nh, c, LANE), 0)
        lane = lax.broadcasted_iota(jnp.int32, (nh, c, LANE), 2)
        dgb_ref[...] = jnp.where(lane == head, dbeta, jnp.where(lane == head + nh, dg, 0.0))

    blk = pl.BlockSpec((c, WIDTH_A), lambda i: (n - 1 - i, 0))
    mat = pl.BlockSpec((nh, 1, c, c), lambda i: (0, n - 1 - i, 0, 0))
    return _call(
        body, name=name, grid=(n,),
        in_specs=[blk, blk, blk, pl.BlockSpec((c, LANE), lambda i: (n - 1 - i, 0)), mat, mat, blk],
        out_specs=[blk, blk, blk, pl.BlockSpec((nh, c, LANE), lambda i: (0, n - 1 - i, 0))],
        out_shape=[jax.ShapeDtypeStruct((t, WIDTH_A), F32)] * 3 + [jax.ShapeDtypeStruct((nh, t, LANE), F32)],
        scratch_shapes=[pltpu.VMEM((nh, HEAD_A, HEAD_A), F32)], sem=("arbitrary",),
        args=(q, k, v, gates, tm_all, s_all, do), comm=comm)


def _onorm_fwd(o, gate, g, name):
    t = o.shape[0]

    def body(o_ref, gate_ref, g_ref, y_ref):
        ov, gv = o_ref[...], gate_ref[...]
        r = lax.rsqrt(jnp.mean(ov * ov, axis=-1, keepdims=True) + RMS_EPS)
        y_ref[...] = (ov * r * g_ref[...] * gv * _sigmoid(gv)).astype(BF16)

    blk = pl.BlockSpec((t, HEAD_A), lambda j: (0, j))
    return pl.pallas_call(
        body, name=name, grid=(N_HEADS_A,), in_specs=[blk, blk, pl.BlockSpec((1, HEAD_A), lambda j: (0, 0))],
        out_specs=blk, out_shape=jax.ShapeDtypeStruct((t, WIDTH_A), BF16),
        compiler_params=_params(("parallel",)))(o, gate, g)


def _onorm_bwd(o, gate, g, dy, name):
    t = o.shape[0]

    def body(o_ref, gate_ref, g_ref, dy_ref, do_ref, dgate_ref, dg_ref):
        @pl.when(pl.program_id(0) == 0)
        def _():
            dg_ref[...] = jnp.zeros_like(dg_ref)

        ov, gv, dyv = o_ref[...], gate_ref[...], dy_ref[...].astype(F32)
        r = lax.rsqrt(jnp.mean(ov * ov, axis=-1, keepdims=True) + RMS_EPS)
        oh = ov * r
        sg, dsg = _silu_and_grad(gv)
        dgate_ref[...] = (dyv * oh * g_ref[...] * dsg).astype(BF16)
        dn = dyv * sg
        dg_ref[...] += jnp.sum(dn * oh, axis=0, keepdims=True)
        dng = dn * g_ref[...]
        do_ref[...] = r * (dng - oh * jnp.mean(dng * oh, axis=-1, keepdims=True))

    blk = pl.BlockSpec((t, HEAD_A), lambda j: (0, j))
    vec = pl.BlockSpec((1, HEAD_A), lambda j: (0, 0))
    return pl.pallas_call(
        body, name=name, grid=(N_HEADS_A,), in_specs=[blk, blk, vec, blk], out_specs=[blk, blk, vec],
        out_shape=[jax.ShapeDtypeStruct((t, WIDTH_A), F32), jax.ShapeDtypeStruct((t, WIDTH_A), BF16),
                   jax.ShapeDtypeStruct((1, HEAD_A), F32)],
        compiler_params=_params(("arbitrary",)))(o, gate, g, dy)


def _cmul(ar, ai, br, bi):
    return ar * br - ai * bi, ar * bi + ai * br


def _scan_tables(ar, ai, reverse):
    p1 = (ar, ai)
    p2 = _cmul(*p1, *p1)
    p4 = _cmul(*p2, *p2)
    p8 = _cmul(*p4, *p4)
    p3 = _cmul(*p2, *p1)
    p5 = _cmul(*p4, *p1)
    p6 = _cmul(*p4, *p2)
    p7 = _cmul(*p4, *p3)
    pows = [p1, p2, p3, p4, p5, p6, p7, p8]
    rows = lax.broadcasted_iota(jnp.int32, (8, ar.shape[1]), 0)
    tr = jnp.zeros((8, ar.shape[1]), F32)
    ti = jnp.zeros((8, ar.shape[1]), F32)
    for r in range(8):
        pw = pows[7 - r] if reverse else pows[r]
        tr = jnp.where(rows == r, pw[0], tr)
        ti = jnp.where(rows == r, pw[1], ti)
    return p1, p2, p4, p8, tr, ti


def _tile_scan(xr, xi, p1, p2, p4, reverse):
    rows = lax.broadcasted_iota(jnp.int32, xr.shape, 0)
    for s, (pr, pi) in ((1, p1), (2, p2), (4, p4)):
        if reverse:
            keep = rows < 8 - s
            sr, si = pltpu.roll(xr, 8 - s, 0), pltpu.roll(xi, 8 - s, 0)
        else:
            keep = rows >= s
            sr, si = pltpu.roll(xr, s, 0), pltpu.roll(xi, s, 0)
        sr, si = jnp.where(keep, sr, 0.0), jnp.where(keep, si, 0.0)
        mr, mi = _cmul(pr, pi, sr, si)
        xr, xi = xr + mr, xi + mi
    return xr, xi


def _s5_scan_fwd(bu, a, name, tb=512, comm=None):
    t = bu.shape[0]
    cb = SCAN_CB
    nt = t // tb

    def body(b_ref, a_ref, x_ref, carry):
        @pl.when(pl.program_id(1) == 0)
        def _():
            carry[...] = jnp.zeros_like(carry)

        ar, ai = a_ref[:, 0:cb], a_ref[:, cb:2 * cb]
        p1, p2, p4, p8, tr, ti = _scan_tables(ar, ai, False)

        def step(j, c):
            cr, ci = c
            i = pl.multiple_of(j * 8, 8)
            xr, xi = _tile_scan(b_ref[pl.ds(i, 8), 0:cb], b_ref[pl.ds(i, 8), cb:2 * cb], p1, p2, p4, False)
            mr, mi = _cmul(tr, ti, cr, ci)
            xr, xi = xr + mr, xi + mi
            x_ref[pl.ds(i, 8), 0:cb] = xr
            x_ref[pl.ds(i, 8), cb:2 * cb] = xi
            return xr[7:8, :], xi[7:8, :]

        cr, ci = lax.fori_loop(0, tb // 8, step, (carry[0:1, :], carry[1:2, :]), unroll=2)
        carry[0:1, :] = cr
        carry[1:2, :] = ci

    blk = pl.BlockSpec((tb, 2 * cb), lambda j, i: (i, j))
    return _call(
        body, name=name, grid=(SSM_CH // cb, nt),
        in_specs=[blk, pl.BlockSpec((1, 2 * cb), lambda j, i: (0, j))], out_specs=blk,
        out_shape=jax.ShapeDtypeStruct((t, 2 * SSM_CH), F32), scratch_shapes=[pltpu.VMEM((8, cb), F32)],
        sem=("parallel", "arbitrary"), args=(bu, a), comm=comm)


def _s5_scan_bwd(dx, x, a, name, tb=512, comm=None):
    t = dx.shape[0]
    cb = SCAN_CB
    nt = t // tb
    nj = tb // 8

    def body(d_ref, x_ref, xp_ref, a_ref, l_ref, da_ref, carry, acc):
        tblk = pl.program_id(1)

        @pl.when(tblk == 0)
        def _():
            carry[...] = jnp.zeros_like(carry)
            acc[...] = jnp.zeros_like(acc)

        ar, ai = a_ref[:, 0:cb], a_ref[:, cb:2 * cb]
        p1, p2, p4, p8, tr, ti = _scan_tables(ar, -ai, True)
        rows = lax.broadcasted_iota(jnp.int32, (8, cb), 0)

        def step(jj, c):
            cr, ci, sr_acc, si_acc = c
            j = nj - 1 - jj
            i = pl.multiple_of(j * 8, 8)
            lr, li = _tile_scan(d_ref[pl.ds(i, 8), 0:cb], d_ref[pl.ds(i, 8), cb:2 * cb], p1, p2, p4, True)
            mr, mi = _cmul(tr, ti, cr, ci)
            lr, li = lr + mr, li + mi
            l_ref[pl.ds(i, 8), 0:cb] = lr
            l_ref[pl.ds(i, 8), cb:2 * cb] = li
            ip = pl.multiple_of(jnp.maximum(j - 1, 0) * 8, 8)
            prev_r = jnp.where(j > 0, x_ref[pl.ds(ip, 8), 0:cb], xp_ref[:, 0:cb])
            prev_i = jnp.where(j > 0, x_ref[pl.ds(ip, 8), cb:2 * cb], xp_ref[:, cb:2 * cb])
            edge = jnp.where(jnp.logical_and(j == 0, tblk == nt - 1), 0.0, 1.0)
            xs_r = jnp.where(rows == 0, pltpu.roll(prev_r, 1, 0) * edge, pltpu.roll(x_ref[pl.ds(i, 8), 0:cb], 1, 0))
            xs_i = jnp.where(rows == 0, pltpu.roll(prev_i, 1, 0) * edge, pltpu.roll(x_ref[pl.ds(i, 8), cb:2 * cb], 1, 0))
            sr_acc = sr_acc + lr * xs_r + li * xs_i
            si_acc = si_acc + li * xs_r - lr * xs_i
            return lr[0:1, :], li[0:1, :], sr_acc, si_acc

        cr, ci, sr_acc, si_acc = lax.fori_loop(
            0, nj, step, (carry[0:1, :], carry[1:2, :], acc[:, 0:cb], acc[:, cb:2 * cb]))
        carry[0:1, :] = cr
        carry[1:2, :] = ci
        acc[:, 0:cb] = sr_acc
        acc[:, cb:2 * cb] = si_acc

        @pl.when(tblk == nt - 1)
        def _():
            da_ref[...] = jnp.sum(acc[...], axis=0, keepdims=True)

    blk = pl.BlockSpec((tb, 2 * cb), lambda j, i: (nt - 1 - i, j))
    prev = pl.BlockSpec((8, 2 * cb), lambda j, i: (jnp.maximum((nt - 1 - i) * (tb // 8) - 1, 0), j))
    vec = pl.BlockSpec((1, 2 * cb), lambda j, i: (0, j))
    return _call(
        body, name=name, grid=(SSM_CH // cb, nt), in_specs=[blk, blk, prev, vec], out_specs=[blk, vec],
        out_shape=[jax.ShapeDtypeStruct((t, 2 * SSM_CH), F32), jax.ShapeDtypeStruct((1, 2 * SSM_CH), F32)],
        scratch_shapes=[pltpu.VMEM((8, cb), F32), pltpu.VMEM((8, 2 * cb), F32)],
        sem=("parallel", "arbitrary"), args=(dx, x, x, a), comm=comm)


def _glu_fwd(yc, u, dvec, wg, bg, name, tr=256):
    t = yc.shape[0]

    def body(yc_ref, u_ref, d_ref, w_ref, b_ref, yl_ref, yb_ref):
        yl = yc_ref[...] + d_ref[...] * u_ref[...]
        yl_ref[...] = yl
        yg, _ = _gelu_and_grad(yl)
        z = jnp.dot(yg.astype(BF16), w_ref[...], preferred_element_type=F32) + b_ref[...]
        yb_ref[...] = (yg * _sigmoid(z)).astype(BF16)

    blk = pl.BlockSpec((tr, SSM_WIDTH), lambda i: (i, 0))
    vec = pl.BlockSpec((1, SSM_WIDTH), lambda i: (0, 0))
    return pl.pallas_call(
        body, name=name, grid=(t // tr,),
        in_specs=[blk, blk, vec, pl.BlockSpec((SSM_WIDTH, SSM_WIDTH), lambda i: (0, 0)), vec],
        out_specs=[blk, blk],
        out_shape=[jax.ShapeDtypeStruct((t, SSM_WIDTH), F32), jax.ShapeDtypeStruct((t, SSM_WIDTH), BF16)],
        compiler_params=_params(("parallel",)))(yc, u, dvec, wg, bg)


def _glu_bwd(yl, u, dvec, wg, bg, dyb, name, tr=256):
    t = yl.shape[0]

    def body(yl_ref, u_ref, d_ref, w_ref, b_ref, dy_ref, dyl_ref, du_ref, dw_ref, db_ref, dd_ref):
        @pl.when(pl.program_id(0) == 0)
        def _():
            dw_ref[...] = jnp.zeros_like(dw_ref)
            db_ref[...] = jnp.zeros_like(db_ref)
            dd_ref[...] = jnp.zeros_like(dd_ref)

        ylv, dyv, wv = yl_ref[...], dy_ref[...].astype(F32), w_ref[...]
        yg, dgelu = _gelu_and_grad(ylv)
        ygb = yg.astype(BF16)
        z = jnp.dot(ygb, wv, preferred_element_type=F32) + b_ref[...]
        sg = _sigmoid(z)
        dz = dyv * yg * sg * (1.0 - sg)
        dzb = dz.astype(BF16)
        dyg = dyv * sg + lax.dot_general(dzb, wv, (((1,), (1,)), ((), ())), preferred_element_type=F32)
        dyl = dyg * dgelu
        dyl_ref[...] = dyl.astype(BF16)
        du_ref[...] = dyl * d_ref[...]
        dw_ref[...] += lax.dot_general(ygb, dzb, (((0,), (0,)), ((), ())), preferred_element_type=F32)
        db_ref[...] += jnp.sum(dz, axis=0, keepdims=True)
        dd_ref[...] += jnp.sum(dyl * u_ref[...], axis=0, keepdims=True)

    blk = pl.BlockSpec((tr, SSM_WIDTH), lambda i: (i, 0))
    vec = pl.BlockSpec((1, SSM_WIDTH), lambda i: (0, 0))
    wsp = pl.BlockSpec((SSM_WIDTH, SSM_WIDTH), lambda i: (0, 0))
    return pl.pallas_call(
        body, name=name, grid=(t // tr,), in_specs=[blk, blk, vec, wsp, vec, blk],
        out_specs=[blk, blk, wsp, vec, vec],
        out_shape=[jax.ShapeDtypeStruct((t, SSM_WIDTH), BF16), jax.ShapeDtypeStruct((t, SSM_WIDTH), F32),
                   jax.ShapeDtypeStruct((SSM_WIDTH, SSM_WIDTH), F32), jax.ShapeDtypeStruct((1, SSM_WIDTH), F32),
                   jax.ShapeDtypeStruct((1, SSM_WIDTH), F32)],
        compiler_params=_params(("arbitrary",)))(yl, u, dvec, wg, bg, dyb)


def _mesh_pos():
    return lax.axis_index("x"), lax.axis_index("y"), lax.axis_index("c")


def _device_index():
    x, y, c = _mesh_pos()
    return 4 * x + 2 * y + c


def _gather_comm(arrays):
    na = len(arrays)

    def own_copy(ins, outs, sems, ai):
        return pltpu.make_async_copy(ins[ai], outs[ai].at[_device_index()], sems[2].at[ai])

    def ctx(ins, outs, sems):
        send_sems, recv_sems = sems[:2]
        x, y, c = _mesh_pos()
        chips = [(1 - x, y), (x, 1 - y), (1 - x, 1 - y)]

        def copy(ai, kk, block, to, own=False):
            slot = outs[ai].at[4 * block[0] + 2 * block[1] + block[2]]
            return pltpu.make_async_remote_copy(
                src_ref=ins[ai] if own else slot, dst_ref=slot, send_sem=send_sems.at[ai, kk],
                recv_sem=recv_sems.at[ai, kk], device_id=to, device_id_type=MESH)

        return (x, y, c), (x, y, 1 - c), chips, c, copy

    def start(ins, outs, sems):
        me, sibling, chips, c, copy = ctx(ins, outs, sems)
        for ai in range(na):
            copy(ai, 0, me, sibling, own=True).start()
            for j, chip in enumerate(chips):
                copy(ai, 1 + j, me, (*chip, c), own=True).start()
        for ai in range(na):
            own_copy(ins, outs, sems, ai).start()

    def mid(ins, outs, sems):
        me, sibling, chips, c, copy = ctx(ins, outs, sems)
        for ai in range(na):
            for j, chip in enumerate(chips):
                copy(ai, 1 + j, (*chip, c), me).wait_recv()
                copy(ai, 4 + j, (*chip, c), sibling).start()

    def end(ins, outs, sems):
        me, sibling, chips, c, copy = ctx(ins, outs, sems)
        for ai in range(na):
            copy(ai, 0, sibling, me).wait_recv()
            copy(ai, 0, me, sibling, own=True).wait_send()
            for j, chip in enumerate(chips):
                copy(ai, 4 + j, (*chip, 1 - c), me).wait_recv()
                copy(ai, 1 + j, me, (*chip, c), own=True).wait_send()
                copy(ai, 4 + j, (*chip, c), sibling).wait_send()
            own_copy(ins, outs, sems, ai).wait()

    return Comm(arrays, [jax.ShapeDtypeStruct((N_DEV,) + a.shape, a.dtype) for a in arrays],
                [pltpu.SemaphoreType.DMA((na, 7)), pltpu.SemaphoreType.DMA((na, 7)), pltpu.SemaphoreType.DMA((na,))],
                start, end, mid)


def _swap_comm(arrays):
    na = len(arrays)
    offs = np.concatenate([[0], np.cumsum([a.shape[1] for a in arrays])]).astype(int)

    def copies(ins, outs, sems):
        x, y, c = _mesh_pos()
        return [pltpu.make_async_remote_copy(
            src_ref=ins[ai].at[2 * k + 1 - c], dst_ref=outs[0].at[k, pl.ds(int(offs[ai]), arrays[ai].shape[1])],
            send_sem=sems[0].at[ai, k], recv_sem=sems[1].at[ai, k], device_id=(x, y, 1 - c), device_id_type=MESH)
            for ai in range(na) for k in range(4)]

    def start(ins, outs, sems):
        for cp in copies(ins, outs, sems):
            cp.start()

    def end(ins, outs, sems):
        for cp in copies(ins, outs, sems):
            cp.wait()

    return Comm(arrays, [jax.ShapeDtypeStruct((4, int(offs[-1]), PACK_COLS), arrays[0].dtype)],
                [pltpu.SemaphoreType.DMA((na, 4)), pltpu.SemaphoreType.DMA((na, 4))], start, end)


def _chips_comm(send):
    def copies(ins, outs, sems):
        x, y, c = _mesh_pos()
        chips = [(1 - x, y), (x, 1 - y), (1 - x, 1 - y)]
        return [pltpu.make_async_remote_copy(
            src_ref=ins[0].at[2 * cx + cy], dst_ref=outs[0].at[j], send_sem=sems[0].at[j], recv_sem=sems[1].at[j],
            device_id=(cx, cy, c), device_id_type=MESH) for j, (cx, cy) in enumerate(chips)]

    def start(ins, outs, sems):
        for cp in copies(ins, outs, sems):
            cp.start()

    def end(ins, outs, sems):
        for cp in copies(ins, outs, sems):
            cp.wait()

    return Comm([send], [jax.ShapeDtypeStruct((3,) + send.shape[1:], send.dtype)],
                [pltpu.SemaphoreType.DMA((3,)), pltpu.SemaphoreType.DMA((3,))], start, end)


def _all_gather(arrays, name):
    na = len(arrays)

    def body(*refs):
        ins, outs = refs[:na], refs[na:2 * na]
        send_sems, recv_sems, local_sems = refs[2 * na:]
        x, y, c = _mesh_pos()
        me, sibling = (x, y, c), (x, y, 1 - c)
        chips = [(1 - x, y), (x, 1 - y), (1 - x, 1 - y)]
        waits = []
        for ai in range(na):
            in_ref, out_ref = ins[ai], outs[ai]

            def slot(px, py, pc, out_ref=out_ref):
                return out_ref.at[4 * px + 2 * py + pc]

            def copy(kk, block, to, src=None, ai=ai, slot=slot):
                return pltpu.make_async_remote_copy(
                    src_ref=slot(*block) if src is None else src, dst_ref=slot(*block),
                    send_sem=send_sems.at[ai, kk], recv_sem=recv_sems.at[ai, kk], device_id=to, device_id_type=MESH)

            mine = pltpu.make_async_copy(in_ref, slot(*me), local_sems.at[ai])
            mine.start()
            first = [copy(0, me, sibling, src=in_ref)]
            first += [copy(1 + j, me, (*chip, c), src=in_ref) for j, chip in enumerate(chips)]
            for cp in first:
                cp.start()
            waits.append((copy, mine, first))
        sends = []
        for ai in range(na):
            copy, mine, first = waits[ai]
            passed = [copy(4 + j, (*chip, c), sibling) for j, chip in enumerate(chips)]
            for j, chip in enumerate(chips):
                copy(1 + j, (*chip, c), me).wait_recv()
                passed[j].start()
            sends.append(passed)
        for ai in range(na):
            copy, mine, first = waits[ai]
            copy(0, sibling, me).wait_recv()
            for j, chip in enumerate(chips):
                copy(4 + j, (*chip, 1 - c), me).wait_recv()
            for cp in first + sends[ai]:
                cp.wait_send()
            mine.wait()

    any_spec = pl.BlockSpec(memory_space=pl.ANY)
    return pl.pallas_call(
        body, name=name, in_specs=[any_spec] * na, out_specs=[any_spec] * na,
        out_shape=[jax.ShapeDtypeStruct((N_DEV,) + a.shape, a.dtype) for a in arrays],
        scratch_shapes=[pltpu.SemaphoreType.DMA((na, 7)), pltpu.SemaphoreType.DMA((na, 7)),
                        pltpu.SemaphoreType.DMA((na,))],
        compiler_params=pltpu.CompilerParams(has_side_effects=True))(*arrays)


def _swap_sibling(arrays, name):
    na = len(arrays)
    offs = np.concatenate([[0], np.cumsum([a.shape[1] for a in arrays])]).astype(int)
    rows = int(offs[-1])

    def body(*refs):
        ins, recv_ref = refs[:na], refs[na]
        send_sems, recv_sems = refs[na + 1:]
        x, y, c = _mesh_pos()
        started = []
        for ai in range(na):
            span = pl.ds(int(offs[ai]), arrays[ai].shape[1])
            for k in range(4):
                remote = pltpu.make_async_remote_copy(
                    src_ref=ins[ai].at[2 * k + 1 - c], dst_ref=recv_ref.at[k, span], send_sem=send_sems.at[ai, k],
                    recv_sem=recv_sems.at[ai, k], device_id=(x, y, 1 - c), device_id_type=MESH)
                remote.start()
                started.append(remote)
        for remote in started:
            remote.wait()

    any_spec = pl.BlockSpec(memory_space=pl.ANY)
    return pl.pallas_call(
        body, name=name, in_specs=[any_spec] * na, out_specs=any_spec,
        out_shape=jax.ShapeDtypeStruct((4, rows, PACK_COLS), arrays[0].dtype),
        scratch_shapes=[pltpu.SemaphoreType.DMA((na, 4)), pltpu.SemaphoreType.DMA((na, 4))])(*arrays)


def _exchange_chips(send, name):
    def body(s_ref, o_ref, send_sems, recv_sems):
        x, y, c = _mesh_pos()
        chips = [(1 - x, y), (x, 1 - y), (1 - x, 1 - y)]
        cps = [pltpu.make_async_remote_copy(
            src_ref=s_ref.at[2 * cx + cy], dst_ref=o_ref.at[j], send_sem=send_sems.at[j], recv_sem=recv_sems.at[j],
            device_id=(cx, cy, c), device_id_type=MESH) for j, (cx, cy) in enumerate(chips)]
        for cp in cps:
            cp.start()
        for cp in cps:
            cp.wait()

    any_spec = pl.BlockSpec(memory_space=pl.ANY)
    return pl.pallas_call(
        body, name=name, in_specs=[any_spec], out_specs=any_spec,
        out_shape=jax.ShapeDtypeStruct((3,) + send.shape[1:], send.dtype),
        scratch_shapes=[pltpu.SemaphoreType.DMA((3,)), pltpu.SemaphoreType.DMA((3,))])(send)


def _pair_sum(keep, recv, name, tr=464):
    nchip, rows, cols = keep.shape

    def body(g_ref, r_ref, o_ref):
        o_ref[...] = (g_ref[...].astype(F32) + r_ref[...].astype(F32)).astype(BF16)

    blk = pl.BlockSpec((1, tr, cols), lambda k, i: (k, i, 0))
    return pl.pallas_call(
        body, name=name, grid=(nchip, rows // tr), in_specs=[blk, blk], out_specs=blk,
        out_shape=jax.ShapeDtypeStruct((nchip, rows, cols), BF16),
        compiler_params=_params(("parallel", "parallel")))(keep, recv)


def _chip_sum(own, others, name, tr=464):
    _, rows, cols = own.shape
    chip = (2 * lax.axis_index("x") + lax.axis_index("y")).astype(jnp.int32).reshape(1)

    def body(chip_ref, own_ref, oth_ref, o_ref):
        del chip_ref
        acc = own_ref[0].astype(F32)
        for j in range(3):
            acc = acc + oth_ref[j].astype(F32)
        o_ref[...] = acc

    grid_spec = pltpu.PrefetchScalarGridSpec(
        num_scalar_prefetch=1, grid=(rows // tr,),
        in_specs=[pl.BlockSpec((1, tr, cols), lambda i, chip_ref: (chip_ref[0], i, 0)),
                  pl.BlockSpec((3, tr, cols), lambda i, chip_ref: (0, i, 0))],
        out_specs=pl.BlockSpec((tr, cols), lambda i, chip_ref: (i, 0)))
    return pl.pallas_call(
        body, name=name, grid_spec=grid_spec, out_shape=jax.ShapeDtypeStruct((rows, cols), F32),
        compiler_params=_params(("parallel",)))(chip, own, others)


def _sum_leading(parts, name, tr=464):
    nparts, rows, cols = parts.shape
    tr = tr if rows % tr == 0 else rows

    def body(p_ref, o_ref):
        acc = p_ref[0].astype(F32)
        for i in range(1, nparts):
            acc = acc + p_ref[i].astype(F32)
        o_ref[...] = acc

    return pl.pallas_call(
        body, name=name, grid=(rows // tr,),
        in_specs=[pl.BlockSpec((nparts, tr, cols), lambda i: (0, i, 0))],
        out_specs=pl.BlockSpec((tr, cols), lambda i: (i, 0)), out_shape=jax.ShapeDtypeStruct((rows, cols), F32),
        compiler_params=_params(("parallel",)))(parts)


def _adamw(w, g, m, v, name, comm=None):
    shape = w.shape
    cols = shape[-1]
    lead = shape[0] if len(shape) >= 3 else 1
    rows = int(np.prod(shape[:-1])) // lead if len(shape) > 1 else 1
    w2, g2, m2, v2 = (a.reshape(lead, rows, cols) for a in (w, g, m, v))
    tr = rows
    for cand in (512, 256, 128, 64, 32, 16, 8):
        if rows % cand == 0 and rows > cand:
            tr = cand
            break
    bc1, bc2 = 1.0 - ADAM_B1 ** ADAM_STEP, 1.0 - ADAM_B2 ** ADAM_STEP

    def body(w_ref, g_ref, m_ref, v_ref, d_ref, nm_ref, nv_ref):
        gv = g_ref[...]
        nm = ADAM_B1 * m_ref[...] + (1.0 - ADAM_B1) * gv
        nv = ADAM_B2 * v_ref[...] + (1.0 - ADAM_B2) * (gv * gv)
        nm_ref[...] = nm
        nv_ref[...] = nv
        d_ref[...] = -ADAM_LR * ((nm / bc1) / (jnp.sqrt(nv / bc2) + ADAM_EPS) + ADAM_WD * w_ref[...])

    blk = pl.BlockSpec((1, tr, cols), lambda l, i: (l, i, 0))
    res = _call(body, name=name, grid=(lead, rows // tr), in_specs=[blk] * 4, out_specs=[blk] * 3,
                out_shape=[jax.ShapeDtypeStruct((lead, rows, cols), F32)] * 3, sem=("parallel", "parallel"),
                args=(w2, g2, m2, v2), comm=comm)
    outs, couts = res if comm is not None else (res, None)
    outs = tuple(o.reshape(shape) for o in outs)
    return outs if comm is None else (outs, couts)


WEIGHT_NAMES = ['norm_mix_g', 'norm_xa_g', 'norm_ffn_g', 'norm_mem_g', 'norm_final_g', 'w_in_ab', 'conv_qkv_a',
                'a_log_a', 'dt_bias_a', 'onorm_g_a', 'ssm_lambda_re', 'ssm_lambda_im', 'ssm_b_re', 'ssm_b_im',
                'ssm_c_re', 'ssm_c_im', 'ssm_d', 'ssm_log_dt', 'w_glu_b', 'b_glu_b', 'w_out_ab', 'pool_w',
                'pool_scale', 'xa_wq', 'xa_wkv', 'xa_wo', 'ffn_w_up', 'ffn_conv', 'ffn_w_down']
BIG_SHARDED = {'w_in_ab': ((1, 1024, 2568), 2), 'w_glu_b': ((1, 512, 512), 1), 'w_out_ab': ((1, 1024, 1024), 1),
               'pool_w': ((1, 4, 256, 256), 2), 'xa_wq': ((2, 1024, 1024), 1), 'xa_wkv': ((2, 1024, 2048), 2),
               'xa_wo': ((2, 1024, 1024), 1), 'ffn_w_up': ((2, 1024, 5632), 2), 'ffn_w_down': ((2, 2816, 1024), 1)}
SMALL_SHARDED = {'conv_qkv_a': ((1, 4, 1536), 2), 'pool_scale': ((1, 1024), 1), 'ffn_conv': ((2, 3, 5632), 2)}
REPLICATED = {'norm_mix_g': (2, 1024), 'norm_xa_g': (2, 1024), 'norm_ffn_g': (2, 1024), 'norm_mem_g': (1024,),
              'norm_final_g': (1024,), 'a_log_a': (1, 4), 'dt_bias_a': (1, 4), 'onorm_g_a': (1, 128),
              'ssm_lambda_re': (1, 32, 64), 'ssm_lambda_im': (1, 32, 64), 'ssm_b_re': (1, 32, 64, 16),
              'ssm_b_im': (1, 32, 64, 16), 'ssm_c_re': (1, 32, 16, 64), 'ssm_c_im': (1, 32, 16, 64),
              'ssm_d': (1, 32, 16), 'ssm_log_dt': (1, 32), 'b_glu_b': (1, 512)}
PACK_ROW_ALIGN = 8


def _shard_shape(shape, axis):
    return tuple(s // N_DEV if i == axis else s for i, s in enumerate(shape))


def _round_up(n, m):
    return (n + m - 1) // m * m


def _pack(arrays):
    total = sum(int(np.prod(a.shape)) for a in arrays)
    padded = _round_up(total, PACK_COLS * PACK_ROW_ALIGN)
    parts = [a.astype(F32).reshape(-1) for a in arrays]
    if padded != total:
        parts.append(jnp.zeros((padded - total,), F32))
    return jnp.concatenate(parts).reshape(padded // PACK_COLS, PACK_COLS)


def _unpack(packed, shapes):
    flat, out, off = packed.reshape(-1), [], 0
    for shape in shapes:
        size = int(np.prod(shape))
        out.append(flat[off:off + size].reshape(shape))
        off += size
    return out


def _split_shards(full, axis):
    shape = full.shape
    s = shape[axis] // N_DEV
    a = full.reshape(shape[:axis] + (N_DEV, s) + shape[axis + 1:])
    return jnp.moveaxis(a, axis, 0).reshape(N_DEV, -1)


def _merge_shards(pieces, shape, axis):
    sh = _shard_shape(shape, axis)
    a = pieces.reshape((N_DEV,) + sh)
    a = jnp.moveaxis(a, 0, axis)
    return a.reshape(shape)


_SCAN_NB = SSM_CH // SCAN_CB


def _to_scan_layout(m, axis):
    shape = m.shape
    m = m.reshape(shape[:axis] + (2, _SCAN_NB, SCAN_CB) + shape[axis + 1:])
    return jnp.swapaxes(m, axis, axis + 1).reshape(shape)


def _from_scan_layout(m, axis):
    shape = m.shape
    m = m.reshape(shape[:axis] + (_SCAN_NB, 2, SCAN_CB) + shape[axis + 1:])
    return jnp.swapaxes(m, axis, axis + 1).reshape(shape)


def _s5_discretise(lam_re, lam_im, b_re, b_im, log_dt):
    dt = jnp.exp(log_dt)[:, None]
    mag = jnp.exp(lam_re * dt)
    ang = lam_im * dt
    lb_re, lb_im = mag * jnp.cos(ang), mag * jnp.sin(ang)
    den = lam_re * lam_re + lam_im * lam_im
    nr, ni = lb_re - 1.0, lb_im
    coef_re = (nr * lam_re + ni * lam_im) / den
    coef_im = (ni * lam_re - nr * lam_im) / den
    bb_re = coef_re[..., None] * b_re - coef_im[..., None] * b_im
    bb_im = coef_re[..., None] * b_im + coef_im[..., None] * b_re
    return lb_re, lb_im, bb_re, bb_im


_GROUPS_PER_BLOCK = N_GROUPS // _SCAN_NB
_U_BLOCK = _GROUPS_PER_BLOCK * SSM_GROUP


def _s5_matrices(lb_re, lb_im, bb_re, bb_im, c_re, c_im):
    eye = jnp.eye(_GROUPS_PER_BLOCK, dtype=F32)
    blocked = lambda m: m.reshape((_SCAN_NB, _GROUPS_PER_BLOCK) + m.shape[1:])
    bmat = lambda bb: jnp.einsum('jgph,gk->jghkp', blocked(bb), eye).reshape(_SCAN_NB, _U_BLOCK, SCAN_CB)
    cmat = lambda cc: jnp.einsum('jghp,gk->jkpgh', blocked(cc), eye).reshape(_SCAN_NB, SCAN_CB, _U_BLOCK)
    b_in = jnp.concatenate([bmat(bb_re), bmat(bb_im)], axis=2)
    c_out = jnp.concatenate([cmat(c_re), -cmat(c_im)], axis=1)
    a_row = _to_scan_layout(jnp.concatenate([lb_re.reshape(1, SSM_CH), lb_im.reshape(1, SSM_CH)], axis=1), 1)
    return b_in, c_out, a_row


def _s5_matrix_grads(db_in, dc_out, da_row):
    da_nat = _from_scan_layout(da_row, 1)
    eye = jnp.eye(_GROUPS_PER_BLOCK, dtype=F32)
    nb, gb = _SCAN_NB, _GROUPS_PER_BLOCK
    bgrad = lambda m: jnp.einsum('jghkp,gk->jgph', m.reshape(nb, gb, SSM_GROUP, gb, SSM_STATE), eye
                                 ).reshape(N_GROUPS, SSM_STATE, SSM_GROUP)
    cgrad = lambda m: jnp.einsum('jkpgh,gk->jghp', m.reshape(nb, gb, SSM_STATE, gb, SSM_GROUP), eye
                                 ).reshape(N_GROUPS, SSM_GROUP, SSM_STATE)
    dbb_re, dbb_im = bgrad(db_in[:, :, :SCAN_CB]), bgrad(db_in[:, :, SCAN_CB:])
    dc_re, dc_im = cgrad(dc_out[:, :SCAN_CB]), -cgrad(dc_out[:, SCAN_CB:])
    dlb_re = da_nat[0, :SSM_CH].reshape(N_GROUPS, SSM_STATE)
    dlb_im = da_nat[0, SSM_CH:].reshape(N_GROUPS, SSM_STATE)
    return dlb_re, dlb_im, dbb_re, dbb_im, dc_re, dc_im


def _as_pieces(a):
    return a.reshape(N_DEV, a.shape[0] // N_DEV, a.shape[1])


def _hybrid_fwd(xn, x, wts, p, weights, riders):
    sv = {}
    hq = _mm(xn, wts['w_qkv_t'], "nt", "l0_in_qkv")
    gate = _mm(xn, wts['w_gate_t'], "nt", "l0_in_gate")
    ba = _mm(xn, wts['w_ba_t'], "nt", "l0_in_ba")
    u = _mm(xn, wts['w_u_t'], "nt", "l0_in_u")
    conv = p['conv_qkv']
    q = _qkv_pre_fwd(hq, conv, 0, 4, True, HEAD_A ** -0.5, "l0_q_pre")
    k = _qkv_pre_fwd(hq, conv, 4, 4, True, 1.0, "l0_k_pre")
    v = _qkv_pre_fwd(hq, conv, 8, 4, False, 1.0, "l0_v_pre")
    gates = _gates_fwd(ba, p['arow'], p['brow'], "l0_gates")
    o, tm_all, s_all = riders.run("l0_gdr_fwd", _gdr_fwd, q, k, v, gates)
    wts['w_glu'], wts['w_out'] = weights.full['w_glu'], weights.full['w_out']
    y_a = _onorm_fwd(o, gate, p['onorm_g'], "l0_onorm")
    bu = riders.run("l0_s5_bu", _mm_bd, u, p['b_in'], "nn")
    xs = riders.run("l0_s5_scan", _s5_scan_fwd, bu, p['a_row'])
    yc = riders.run("l0_s5_cx", _mm_bd, xs, p['c_out'], "nn")
    yl, y_b = _glu_fwd(yc, u, p['d_row'], wts['w_glu'], p['b_glu'], "l0_glu")
    mixed = jnp.concatenate([y_a, y_b], axis=1)
    x1 = _mm(mixed, wts['w_out'], "nn", "l0_out", res=x)
    sv.update(hq=hq, gate=gate, ba=ba, u=u, q=q, k=k, v=v, gb=gates, o=o, tm=tm_all, s=s_all, xs=xs, yl=yl, mixed=mixed)
    return x1, sv


def _hybrid_bwd(dx1, xn, wts, p, sv, riders):
    gr = {}
    dmixed = _mm(dx1, wts['w_out'], "nt", "l0_out_dx", out_dtype=BF16)
    riders.grad('w_out', _as_pieces(_mm(sv['mixed'], dx1, "tn", "l0_out_dw", out_dtype=BF16)))
    dya, dyb = dmixed[:, :WIDTH_A], dmixed[:, WIDTH_A:]
    dyl, du_direct, dw_glu, gr['b_glu_b'], dd = _glu_bwd(
        sv['yl'], sv['u'], p['d_row'], wts['w_glu'], p['b_glu'], dyb, "l0_glu_bwd")
    riders.grad('w_glu', dw_glu.astype(BF16).reshape(N_DEV, -1, PACK_COLS))
    dxs = riders.run("l0_s5_cx_dx", _mm_bd, dyl, p['c_out'], "nt")
    dc_out = _mm_bd(sv['xs'], dyl, "tn", "l0_s5_cx_dw")
    lam, da_row = riders.run("l0_s5_scan_bwd", _s5_scan_bwd, dxs, sv['xs'], p['a_row'])
    du = _mm_bd(lam, p['b_in'], "nt", "l0_s5_bu_dx", res=du_direct, out_dtype=BF16)
    db_in = _mm_bd(sv['u'], lam, "tn", "l0_s5_bu_dw")
    gr['s5'] = (db_in, dc_out, da_row, dd)
    do, dgate, gr['onorm_g_a'] = _onorm_bwd(sv['o'], sv['gate'], p['onorm_g'], dya, "l0_onorm_bwd")
    dq, dk, dv, dgb = riders.run("l0_gdr_bwd", _gdr_bwd, sv['q'], sv['k'], sv['v'], sv['gb'], sv['tm'], sv['s'], do)
    conv = p['conv_qkv']
    dhq_q, dcw_q = _qkv_pre_bwd(sv['hq'], conv, dq, 0, 4, True, HEAD_A ** -0.5, "l0_q_pre_bwd")
    dhq_k, dcw_k = _qkv_pre_bwd(sv['hq'], conv, dk, 4, 4, True, 1.0, "l0_k_pre_bwd")
    dhq_v, dcw_v = _qkv_pre_bwd(sv['hq'], conv, dv, 8, 4, False, 1.0, "l0_v_pre_bwd")
    gr['conv_qkv_a'] = jnp.concatenate([dcw_q, dcw_k, dcw_v], axis=1)
    dhq = jnp.concatenate([dhq_q, dhq_k, dhq_v], axis=1)
    dba, da_log, ddt_bias = _gates_bwd(sv['ba'], p['arow'], p['brow'], dgb, "l0_gates_bwd")
    gr['a_log_a'], gr['dt_bias_a'] = da_log[:, 4:8], ddt_bias[:, 4:8]
    dw_qkv_t = _mm(dhq, xn, "tn", "l0_in_qkv_dw", out_dtype=BF16)
    dw_gate_t = _mm(dgate, xn, "tn", "l0_in_gate_dw", out_dtype=BF16)
    dw_ba_t = _mm(dba, xn, "tn", "l0_in_ba_dw", out_dtype=BF16)
    dw_u_t = _mm(du, xn, "tn", "l0_in_u_dw", out_dtype=BF16)
    dw_in_t = _as_pieces(jnp.concatenate([dw_qkv_t, dw_gate_t, dw_ba_t[:8], dw_u_t], axis=0))
    riders.grad('w_in_t', jnp.concatenate(
        [dw_in_t, jnp.zeros((N_DEV, dict(PIECES)['w_in_t'] - W_IN_PIECE, D_MODEL), BF16)], axis=1))
    dxn = riders.run("l0_in_qkv_dx", _mm, dhq, wts['w_qkv_t'], "nn")
    dxn = _mm(dgate, wts['w_gate_t'], "nn", "l0_in_gate_dx", res=dxn)
    dxn = _mm(dba, wts['w_ba_t'], "nn", "l0_in_ba_dx", res=dxn)
    dxn = _mm(du, wts['w_u_t'], "nn", "l0_in_u_dx", res=dxn)
    return dxn, gr


def _xa_fwd(x1, g, mem_n, wq, wkv_t, wo, tag, riders):
    xq = _rms_fwd(x1, g, BF16, tag + "_norm")
    q = _mm(xq, wq, "nn", tag + "_q", out_dtype=BF16)
    kv = _mm(mem_n, wkv_t, "nt", tag + "_kv", out_dtype=BF16)
    o = riders.run(tag + "_attn", _attn_fwd, q, kv)
    x2 = _mm(o, wo, "nn", tag + "_o", res=x1)
    return x2, dict(xq=xq, q=q, kv=kv, o=o)


def _xa_bwd(dx2, x1, g, mem_n, wq, wkv_t, wo, sv, tag, layer, riders):
    do = _mm(dx2, wo, "nt", tag + "_o_dx", out_dtype=BF16)
    riders.grad('wo%d' % layer, _as_pieces(_mm(sv['o'], dx2, "tn", tag + "_o_dw", out_dtype=BF16)))
    dq, dk, dv = _attn_bwd(sv['q'], sv['kv'], do, tag + "_attn_bwd")
    dkv = jnp.concatenate([dk, dv], axis=1).astype(BF16)
    dxq = _mm(dq, wq, "nt", tag + "_q_dx")
    riders.grad('wq%d' % layer, _as_pieces(_mm(sv['xq'], dq, "tn", tag + "_q_dw", out_dtype=BF16)))
    dmem_n = _mm(dkv, wkv_t, "nn", tag + "_kv_dx")
    riders.grad('wkv_t%d' % layer, _as_pieces(_mm(dkv, mem_n, "tn", tag + "_kv_dw", out_dtype=BF16)))
    dx1, dg = riders.run(tag + "_norm_bwd", _rms_bwd, x1, g, dxq, dx2)
    return dx1, dmem_n, dg


def _ffn_fwd(x2, g, w_up_t, conv, w_down, tag, riders):
    xf = _rms_fwd(x2, g, BF16, tag + "_norm")
    h = riders.run(tag + "_up", _mm, xf, w_up_t, "nt")
    a = riders.run(tag + "_act", _ffn_act_fwd, h, conv)
    x3 = _mm(a, w_down, "nn", tag + "_down", res=x2)
    return x3, dict(xf=xf, h=h, a=a)


def _ffn_bwd(dx3, x2, g, w_up_t, conv, w_down, sv, tag, layer, riders):
    da = _mm(dx3, w_down, "nt", tag + "_down_dx")
    riders.grad('down%d' % layer, _as_pieces(_mm(sv['a'], dx3, "tn", tag + "_down_dw", out_dtype=BF16)))
    dh, dconv = riders.run(tag + "_act_bwd", _ffn_act_bwd, sv['h'], conv, da)
    dxf = riders.run(tag + "_up_dx", _mm, dh, w_up_t, "nn")
    dw_up_t = riders.run(tag + "_up_dw", _mm, dh, sv['xf'], "tn", out_dtype=BF16)
    riders.grad('up_t%d' % layer, _as_pieces(dw_up_t))
    dx2, dg = riders.run(tag + "_norm_bwd", _rms_bwd, x2, g, dxf, dx3)
    return dx2, dconv, dg


BIG_NAMES, SMALL_NAMES, REP_NAMES = list(BIG_SHARDED), list(SMALL_SHARDED), list(REPLICATED)
BIG_SIZES = [int(np.prod(_shard_shape(*BIG_SHARDED[n]))) for n in BIG_NAMES]
SMALL_SIZES = [int(np.prod(_shard_shape(*SMALL_SHARDED[n]))) for n in SMALL_NAMES]


PIECES = [('w_in_t', 384), ('w_glu', 32), ('w_out', 128), ('pool_w', 32), ('wq0', 128), ('wq1', 128),
          ('wkv_t0', 256), ('wkv_t1', 256), ('wo0', 128), ('wo1', 128), ('up_t0', 704), ('up_t1', 704),
          ('down0', 352), ('down1', 352)]
PIECE_OFFS = dict(zip([k for k, _ in PIECES], np.concatenate([[0], np.cumsum([r for _, r in PIECES])[:-1]]).tolist()))
W_IN_ROWS = 4 * WIDTH_A + 2 * N_HEADS_A + SSM_WIDTH
W_IN_PIECE = W_IN_ROWS // N_DEV


def _row_tile(rows):
    return max(t for t in range(16, min(rows, 512) + 1, 16) if rows % t == 0)


class _Riders:
    def __init__(self):
        self.waiting = {}
        self.grads = {}
        self.groups = []
        self.reduced = {}

    def add(self, host, comm, then):
        self.waiting.setdefault(host, []).append((comm, then))

    def run(self, name, fn, *args, **kw):
        riders = self.waiting.pop(name, [])
        if not riders:
            return fn(*args, name=name, **kw)
        out, couts = fn(*args, name=name, comm=[c for c, _ in riders], **kw)
        for (_, then), got in zip(riders, couts):
            then(got)
        return out

    def exchange(self, comm, host, name, then):
        if host is None:
            then(_comm_only(comm, name))
        else:
            self.add(host, comm, then)

    def grad(self, key, pieces):
        self.grads[key] = pieces
        for group in [g for g in self.groups if all(k in self.grads for k in g[1])]:
            self.groups.remove(group)
            self._reduce(*group)

    def _reduce(self, name, keys, swap_host, chips_host):
        arrays = [self.grads[k] for k in keys]
        rows = sum(a.shape[1] for a in arrays)
        tile = _row_tile(rows)

        def after_chips(chip_sums, got):
            total = _chip_sum(chip_sums, got[0], name + "_chip_sum", tr=tile)
            off = 0
            for k, a in zip(keys, arrays):
                self.reduced[k] = total[off:off + a.shape[1]]
                off += a.shape[1]

        def after_swap(got):
            core = lax.axis_index("c")
            keep = jnp.concatenate(
                [lax.dynamic_index_in_dim(a.reshape(4, 2, a.shape[1], PACK_COLS), core, 1, keepdims=False)
                 for a in arrays], axis=1)
            chip_sums = _pair_sum(keep, got[0], name + "_pair_sum", tr=tile)
            self.exchange(_chips_comm(chip_sums), chips_host, name + "_to_chips",
                          functools.partial(after_chips, chip_sums))

        self.exchange(_swap_comm(arrays), swap_host, name + "_to_sibling", after_swap)


class _Weights:
    def __init__(self, inp):
        bf = lambda a: a.astype(BF16)
        local = {'w_in_t': bf(inp['w_in_ab'][0]).T, 'w_glu': bf(inp['w_glu_b'][0]), 'w_out': bf(inp['w_out_ab'][0]),
                 'pool_w': bf(inp['pool_w'][0]),
                 'small': _pack([inp[n] for n in SMALL_NAMES])}
        for l in range(2):
            local['wq%d' % l] = bf(inp['xa_wq'][l])
            local['wkv_t%d' % l] = bf(inp['xa_wkv'][l]).T
            local['wo%d' % l] = bf(inp['xa_wo'][l])
            local['up_t%d' % l] = bf(inp['ffn_w_up'][l]).T
            local['down%d' % l] = bf(inp['ffn_w_down'][l])
        self.local, self.full = local, {}

    def plan(self, keys):
        return _gather_comm([self.local[k] for k in keys])

    def land(self, keys, gathered):
        for k, g in zip(keys, gathered):
            if k == 'small':
                off = 0
                for n, size in zip(SMALL_NAMES, SMALL_SIZES):
                    self.full[n] = _merge_shards(g.reshape(N_DEV, -1)[:, off:off + size], *SMALL_SHARDED[n])
                    off += size
            elif k == 'pool_w':
                self.full[k] = jnp.swapaxes(g, 0, 1).reshape(len(POOL_WINDOWS), POOL_GROUP, POOL_GROUP)
            else:
                self.full[k] = g.reshape(N_DEV * g.shape[1], g.shape[2])


GATHER_FIRST = ['w_in_t', 'small']
GATHER_RIDES = [('l0_gdr_fwd', ['w_glu', 'w_out', 'wq0', 'wkv_t0', 'wo0', 'up_t0']),
                ('l0_s5_bu', ['pool_w', 'wq1']), ('l0_s5_scan', ['down0']), ('l0_s5_cx', ['wo1']),
                ('l0_xa_attn', ['wkv_t1']), ('l0_ffn_up', ['up_t1']), ('l0_ffn_act', ['down1'])]
GRAD_RIDES = [('g_down1', ['down1'], 'l1_ffn_act_bwd', 'l1_ffn_up_dx'),
              ('g_up1', ['up_t1'], 'l1_ffn_norm_bwd', 'l0_ffn_act_bwd'),
              ('g_xa1', ['wq1', 'wkv_t1', 'wo1', 'pool_w'], 'l1_mix_norm_bwd', 'l0_ffn_up_dx'),
              ('g_down0', ['down0'], 'l0_ffn_act_bwd', 'l0_ffn_up_dw'),
              ('g_up0', ['up_t0'], 'l0_ffn_norm_bwd', 'l0_gdr_bwd'),
              ('g_xa0', ['wq0', 'wkv_t0', 'wo0'], 'l0_xa_norm_bwd', 'l0_gdr_bwd'),
              ('g_out', ['w_out', 'w_glu'], 'l0_s5_cx_dx', 'l0_s5_scan_bwd'),
              ('g_in', ['w_in_t'], 'l0_in_qkv_dx', 'l0_mix_norm_bwd')]


def _local_step(inp):
    f32_of = lambda n: inp[n].astype(F32)
    weights = _Weights(inp)
    riders = _Riders()
    riders.groups = list(GRAD_RIDES)
    full = weights.full
    weights.land(GATHER_FIRST, _comm_only(weights.plan(GATHER_FIRST), "gather_first"))
    for host, keys in GATHER_RIDES:
        riders.add(host, weights.plan(keys), functools.partial(weights.land, keys))
    w_in_t = full['w_in_t']
    wts0 = dict(w_qkv_t=w_in_t[:3 * WIDTH_A], w_gate_t=w_in_t[3 * WIDTH_A:4 * WIDTH_A],
                w_ba_t=jnp.concatenate([w_in_t[4 * WIDTH_A:4 * WIDTH_A + 8], jnp.zeros((LANE - 8, D_MODEL), BF16)], 0),
                w_u_t=w_in_t[4 * WIDTH_A + 8:])
    lb_disc, disc_vjp = jax.vjp(_s5_discretise, f32_of('ssm_lambda_re')[0], f32_of('ssm_lambda_im')[0],
                                f32_of('ssm_b_re')[0], f32_of('ssm_b_im')[0], f32_of('ssm_log_dt')[0])
    b_in, c_out, a_row = _s5_matrices(*lb_disc, f32_of('ssm_c_re')[0], f32_of('ssm_c_im')[0])
    zeros4 = jnp.zeros((1, 4), F32)
    p0 = dict(conv_qkv=full['conv_qkv_a'][0], onorm_g=f32_of('onorm_g_a'),
              arow=jnp.concatenate([zeros4, f32_of('a_log_a'), jnp.zeros((1, LANE - 8), F32)], 1),
              brow=jnp.concatenate([zeros4, f32_of('dt_bias_a'), jnp.zeros((1, LANE - 8), F32)], 1),
              b_in=b_in.astype(BF16), c_out=c_out.astype(BF16), a_row=a_row,
              d_row=f32_of('ssm_d').reshape(1, SSM_WIDTH), b_glu=f32_of('b_glu_b'))

    x0 = inp['x'][0]
    mem_n = _rms_fwd(inp['mem'][0], inp['norm_mem_g'], BF16, "mem_norm")
    xn0 = _rms_fwd(x0, inp['norm_mix_g'][0], BF16, "l0_mix_norm")
    x1, sv_mix0 = _hybrid_fwd(xn0, x0, wts0, p0, weights, riders)
    x2, sv_xa0 = _xa_fwd(x1, inp['norm_xa_g'][0], mem_n, full['wq0'], full['wkv_t0'], full['wo0'], "l0_xa", riders)
    x3, sv_ffn0 = _ffn_fwd(x2, inp['norm_ffn_g'][0], full['up_t0'], full['ffn_conv'][0], full['down0'], "l0_ffn", riders)
    xn1 = _rms_fwd(x3, inp['norm_mix_g'][1], F32, "l1_mix_norm")
    x4 = _pool_fwd(xn1, full['pool_w'], full['pool_scale'], x3, "l1_pool")
    x5, sv_xa1 = _xa_fwd(x4, inp['norm_xa_g'][1], mem_n, full['wq1'], full['wkv_t1'], full['wo1'], "l1_xa", riders)
    x6, sv_ffn1 = _ffn_fwd(x5, inp['norm_ffn_g'][1], full['up_t1'], full['ffn_conv'][1], full['down1'], "l1_ffn", riders)
    loss_part, dx6, dg_final = _loss_head(x6, inp['norm_final_g'], inp['loss_target'][0], "loss_head")

    dx5, dconv1, dg_ffn1 = _ffn_bwd(dx6, x5, inp['norm_ffn_g'][1], full['up_t1'], full['ffn_conv'][1], full['down1'],
                                    sv_ffn1, "l1_ffn", 1, riders)
    dx4, dmem1, dg_xa1 = _xa_bwd(dx5, x4, inp['norm_xa_g'][1], mem_n, full['wq1'], full['wkv_t1'], full['wo1'],
                                 sv_xa1, "l1_xa", 1, riders)
    dxn1, dpool_w, dpool_scale = _pool_bwd(xn1, full['pool_w'], full['pool_scale'], dx4, "l1_pool_bwd")
    pool_pieces = jnp.swapaxes(dpool_w.astype(BF16).reshape(len(POOL_WINDOWS), N_DEV, -1, POOL_GROUP), 0, 1)
    riders.grad('pool_w', pool_pieces.reshape(N_DEV, -1, PACK_COLS))
    dx3, dg_mix1 = riders.run("l1_mix_norm_bwd", _rms_bwd, x3, inp['norm_mix_g'][1], dxn1, dx4)
    dx2, dconv0, dg_ffn0 = _ffn_bwd(dx3, x2, inp['norm_ffn_g'][0], full['up_t0'], full['ffn_conv'][0], full['down0'],
                                    sv_ffn0, "l0_ffn", 0, riders)
    dx1, dmem0, dg_xa0 = _xa_bwd(dx2, x1, inp['norm_xa_g'][0], mem_n, full['wq0'], full['wkv_t0'], full['wo0'],
                                 sv_xa0, "l0_xa", 0, riders)
    dxn0, g_mix0 = _hybrid_bwd(dx1, xn0, wts0, p0, sv_mix0, riders)
    grad_x, dg_mix0 = riders.run("l0_mix_norm_bwd", _rms_bwd, x0, inp['norm_mix_g'][0], dxn0, dx1)
    _, dg_mem = _rms_bwd(inp['mem'][0], inp['norm_mem_g'], dmem0 + dmem1, None, "mem_norm_bwd")
    assert not riders.waiting and not riders.groups, (list(riders.waiting), riders.groups)

    db_in, dc_out, da_row, dd = g_mix0['s5']
    dlb_re, dlb_im, dbb_re, dbb_im, dc_re, dc_im = _s5_matrix_grads(db_in, dc_out, da_row)
    dlam_re, dlam_im, dbr, dbi, dlog_dt = disc_vjp((dlb_re, dlb_im, dbb_re, dbb_im))

    rep_grads = {
        'norm_mix_g': jnp.concatenate([dg_mix0, dg_mix1], 0), 'norm_xa_g': jnp.concatenate([dg_xa0, dg_xa1], 0),
        'norm_ffn_g': jnp.concatenate([dg_ffn0, dg_ffn1], 0), 'norm_mem_g': dg_mem.reshape(-1),
        'norm_final_g': dg_final.reshape(-1), 'a_log_a': g_mix0['a_log_a'], 'dt_bias_a': g_mix0['dt_bias_a'],
        'onorm_g_a': g_mix0['onorm_g_a'], 'ssm_lambda_re': dlam_re[None], 'ssm_lambda_im': dlam_im[None],
        'ssm_b_re': dbr[None], 'ssm_b_im': dbi[None], 'ssm_c_re': dc_re[None], 'ssm_c_im': dc_im[None],
        'ssm_d': dd.reshape(1, N_GROUPS, SSM_GROUP), 'ssm_log_dt': dlog_dt[None], 'b_glu_b': g_mix0['b_glu_b']}
    small_grads = {'conv_qkv_a': g_mix0['conv_qkv_a'][None], 'pool_scale': dpool_scale,
                   'ffn_conv': jnp.stack([dconv0, dconv1])}
    for key, rows in PIECES:
        assert riders.reduced[key].shape == (rows, PACK_COLS), key
    return loss_part, grad_x, riders.reduced, rep_grads, small_grads


SMALL_GRADS_RIDE_ON = "adamw_ffn_w_up"


def _update(inp, loss_part, grad_x, big_reduced, rep_grads, small_grads):
    dev = _device_index()
    riders, gathered = _Riders(), {}
    misc_local = _pack([rep_grads[n] for n in REP_NAMES] + [small_grads[n] for n in SMALL_NAMES])
    riders.add(SMALL_GRADS_RIDE_ON, _gather_comm([misc_local]), lambda got: gathered.update(misc=got[0]))
    piece = lambda key, rows=None: big_reduced[key] if rows is None else big_reduced[key][:rows]
    both = lambda name: jnp.stack([piece(name + '0'), piece(name + '1')])
    swap = lambda a: jnp.swapaxes(a, -1, -2)
    reduced = {'w_in_ab': piece('w_in_t', W_IN_PIECE)[None], 'w_glu_b': piece('w_glu').reshape(inp['w_glu_b'].shape),
               'w_out_ab': piece('w_out')[None], 'pool_w': piece('pool_w').reshape(inp['pool_w'].shape),
               'xa_wq': both('wq'), 'xa_wkv': both('wkv_t'), 'xa_wo': both('wo'),
               'ffn_w_up': both('up_t'), 'ffn_w_down': both('down')}
    transposed = ('w_in_ab', 'xa_wkv', 'ffn_w_up')
    grads, upd = {}, {}
    for n in sorted(BIG_NAMES, key=lambda n: "adamw_" + n != SMALL_GRADS_RIDE_ON):
        fix = swap if n in transposed else (lambda a: a)
        out = riders.run("adamw_" + n, _adamw, fix(inp[n]), reduced[n], fix(inp['m_' + n]), fix(inp['v_' + n]))
        upd[n], grads[n] = tuple(fix(o) for o in out), fix(reduced[n])
    misc_sum = _sum_leading(gathered['misc'], "small_grads_sum")
    misc = _unpack(misc_sum, [inp[n].shape for n in REP_NAMES] + [SMALL_SHARDED[n][0] for n in SMALL_NAMES])
    for n, g in zip(REP_NAMES, misc):
        grads[n] = g
    for n, g in zip(SMALL_NAMES, misc[len(REP_NAMES):]):
        grads[n] = lax.dynamic_index_in_dim(_split_shards(g, SMALL_SHARDED[n][1]), dev, 0, keepdims=False
                                            ).reshape(inp[n].shape)
    tiny_names = REP_NAMES + SMALL_NAMES
    rep_total = sum(int(np.prod(inp[n].shape)) for n in REP_NAMES)
    packs = [_pack([inp[prefix + n] for n in tiny_names]) for prefix in ('', 'm_', 'v_')]
    g_pack = _pack([misc_sum.reshape(-1)[:rep_total]] + [grads[n] for n in SMALL_NAMES])
    tiny_out = [_unpack(o, [inp[n].shape for n in tiny_names])
                for o in _adamw(packs[0], g_pack, packs[1], packs[2], "adamw_small")]
    for i, n in enumerate(tiny_names):
        upd[n] = tuple(o[i] for o in tiny_out)

    loss = lax.psum(loss_part[0, 0], ("x", "y", "c"))
    outs = [loss, grad_x[None]]
    outs += [grads[n] for n in WEIGHT_NAMES]
    for i in range(3):
        outs += [upd[n][i] for n in WEIGHT_NAMES]
    return tuple(outs)


def _step(inp):
    loss_part, grad_x, big_reduced, rep_grads, small_grads = _local_step(inp)
    return _update(inp, loss_part, grad_x, big_reduced, rep_grads, small_grads)


INPUT_NAMES = (['x', 'mem'] + WEIGHT_NAMES + ['loss_target'] + ['m_' + n for n in WEIGHT_NAMES]
               + ['v_' + n for n in WEIGHT_NAMES])


def kernel(x, mem, norm_mix_g, norm_xa_g, norm_ffn_g, norm_mem_g, norm_final_g, w_in_ab, conv_qkv_a, a_log_a, dt_bias_a, onorm_g_a, ssm_lambda_re, ssm_lambda_im, ssm_b_re, ssm_b_im, ssm_c_re, ssm_c_im, ssm_d, ssm_log_dt, w_glu_b, b_glu_b, w_out_ab, pool_w, pool_scale, xa_wq, xa_wkv, xa_wo, ffn_w_up, ffn_conv, ffn_w_down, loss_target, m_norm_mix_g, m_norm_xa_g, m_norm_ffn_g, m_norm_mem_g, m_norm_final_g, m_w_in_ab, m_conv_qkv_a, m_a_log_a, m_dt_bias_a, m_onorm_g_a, m_ssm_lambda_re, m_ssm_lambda_im, m_ssm_b_re, m_ssm_b_im, m_ssm_c_re, m_ssm_c_im, m_ssm_d, m_ssm_log_dt, m_w_glu_b, m_b_glu_b, m_w_out_ab, m_pool_w, m_pool_scale, m_xa_wq, m_xa_wkv, m_xa_wo, m_ffn_w_up, m_ffn_conv, m_ffn_w_down, v_norm_mix_g, v_norm_xa_g, v_norm_ffn_g, v_norm_mem_g, v_norm_final_g, v_w_in_ab, v_conv_qkv_a, v_a_log_a, v_dt_bias_a, v_onorm_g_a, v_ssm_lambda_re, v_ssm_lambda_im, v_ssm_b_re, v_ssm_b_im, v_ssm_c_re, v_ssm_c_im, v_ssm_d, v_ssm_log_dt, v_w_glu_b, v_b_glu_b, v_w_out_ab, v_pool_w, v_pool_scale, v_xa_wq, v_xa_wkv, v_xa_wo, v_ffn_w_up, v_ffn_conv, v_ffn_w_down):
    args = (x, mem, norm_mix_g, norm_xa_g, norm_ffn_g, norm_mem_g, norm_final_g, w_in_ab, conv_qkv_a, a_log_a, dt_bias_a, onorm_g_a, ssm_lambda_re, ssm_lambda_im, ssm_b_re, ssm_b_im, ssm_c_re, ssm_c_im, ssm_d, ssm_log_dt, w_glu_b, b_glu_b, w_out_ab, pool_w, pool_scale, xa_wq, xa_wkv, xa_wo, ffn_w_up, ffn_conv, ffn_w_down, loss_target, m_norm_mix_g, m_norm_xa_g, m_norm_ffn_g, m_norm_mem_g, m_norm_final_g, m_w_in_ab, m_conv_qkv_a, m_a_log_a, m_dt_bias_a, m_onorm_g_a, m_ssm_lambda_re, m_ssm_lambda_im, m_ssm_b_re, m_ssm_b_im, m_ssm_c_re, m_ssm_c_im, m_ssm_d, m_ssm_log_dt, m_w_glu_b, m_b_glu_b, m_w_out_ab, m_pool_w, m_pool_scale, m_xa_wq, m_xa_wkv, m_xa_wo, m_ffn_w_up, m_ffn_conv, m_ffn_w_down, v_norm_mix_g, v_norm_xa_g, v_norm_ffn_g, v_norm_mem_g, v_norm_final_g, v_w_in_ab, v_conv_qkv_a, v_a_log_a, v_dt_bias_a, v_onorm_g_a, v_ssm_lambda_re, v_ssm_lambda_im, v_ssm_b_re, v_ssm_b_im, v_ssm_c_re, v_ssm_c_im, v_ssm_d, v_ssm_log_dt, v_w_glu_b, v_b_glu_b, v_w_out_ab, v_pool_w, v_pool_scale, v_xa_wq, v_xa_wkv, v_xa_wo, v_ffn_w_up, v_ffn_conv, v_ffn_w_down)
    return _step(dict(zip(INPUT_NAMES, args)))
```

```python
import functools
import math

import numpy as np
import jax
import jax.numpy as jnp
from jax import lax
from jax.experimental import pallas as pl
from jax.experimental.pallas import tpu as pltpu

F32, BF16 = jnp.float32, jnp.bfloat16
HIGH, HIGHEST = lax.Precision.HIGH, lax.Precision.HIGHEST
MESH = pl.DeviceIdType.MESH

N_DEV = 8
SEQ, D_MODEL, MEM_LEN = 2048, 1024, 256
WIDTH_A, N_HEADS_A, HEAD_A, CONV_A = 512, 4, 128, 4
GDR_CHUNK = 128
GDR_HEADS_PER_STEP = 4
SSM_WIDTH, SSM_GROUP, N_GROUPS, SSM_STATE = 512, 16, 32, 64
SSM_CH = N_GROUPS * SSM_STATE
SCAN_CB = 512
POOL_WINDOWS = (2, 4, 8, 16)
POOL_GROUP = 256
N_HEADS_X, HEAD_X = 4, 256
D_FF, CONV_FFN = 2816, 3
RMS_EPS = 1e-6
ADAM_LR, ADAM_B1, ADAM_B2, ADAM_EPS, ADAM_WD, ADAM_STEP = 0.001, 0.9, 0.999, 1e-08, 0.01, 10
LANE = 128
PACK_COLS = 1024
VMEM_LIMIT_BYTES = 56 * 1024 * 1024


def _params(sem=None):
    return pltpu.CompilerParams(dimension_semantics=sem, vmem_limit_bytes=VMEM_LIMIT_BYTES)


class Comm:
    def __init__(self, inputs, out_shapes, sems, start, end, mid=None):
        self.inputs, self.out_shapes, self.sems = list(inputs), list(out_shapes), list(sems)
        self.start, self.mid, self.end = start, mid, end


def _merge_comms(comms):
    comms = [c for c in comms if c is not None]
    if not comms:
        return None, []
    bounds, ni, no, ns = [], 0, 0, 0
    for c in comms:
        bounds.append((ni, no, ns))
        ni, no, ns = ni + len(c.inputs), no + len(c.out_shapes), ns + len(c.sems)

    def phase(which):
        def run(ins, outs, sems):
            for c, (i0, o0, s0) in zip(comms, bounds):
                fn = getattr(c, which)
                if fn is not None:
                    fn(ins[i0:i0 + len(c.inputs)], outs[o0:o0 + len(c.out_shapes)], sems[s0:s0 + len(c.sems)])
        return run

    merged = Comm([a for c in comms for a in c.inputs], [s for c in comms for s in c.out_shapes],
                  [s for c in comms for s in c.sems], phase("start"), phase("end"), phase("mid"))
    return merged, [(o0, o0 + len(c.out_shapes)) for c, (_, o0, _) in zip(comms, bounds)]


def _call(body, *, name, grid, in_specs, out_specs, out_shape, args, scratch_shapes=(), sem=None, comm=None):
    single = not isinstance(out_shape, (list, tuple))
    out_specs_l = [out_specs] if single else list(out_specs)
    out_shape_l = [out_shape] if single else list(out_shape)
    scratch_shapes = list(scratch_shapes)
    merged, spans = _merge_comms(comm if isinstance(comm, (list, tuple)) else [comm])
    if merged is None:
        outs = pl.pallas_call(body, name=name, grid=grid, in_specs=list(in_specs), out_specs=out_specs_l,
                              out_shape=out_shape_l, scratch_shapes=scratch_shapes, compiler_params=_params(sem))(*args)
        outs = outs[0] if single else outs
        return outs if comm is None else (outs, [])
    n_in, n_out, n_scr = len(in_specs), len(out_specs_l), len(scratch_shapes)
    ci, co = len(merged.inputs), len(merged.out_shapes)
    total = int(np.prod(grid))

    def wrapped(*refs):
        ins, cins = refs[:n_in], refs[n_in:n_in + ci]
        outs, couts = refs[n_in + ci:n_in + ci + n_out], refs[n_in + ci + n_out:n_in + ci + n_out + co]
        scr, csems = refs[n_in + ci + n_out + co:n_in + ci + n_out + co + n_scr], refs[n_in + ci + n_out + co + n_scr:]
        lin = pl.program_id(0)
        for d in range(1, len(grid)):
            lin = lin * grid[d] + pl.program_id(d)
        pl.when(lin == 0)(lambda: merged.start(cins, couts, csems))
        body(*ins, *outs, *scr)
        mid_step = min((3 * total) // 4, total - 1)
        pl.when(lin == mid_step)(lambda: merged.mid(cins, couts, csems))
        pl.when(lin == total - 1)(lambda: merged.end(cins, couts, csems))

    any_spec = pl.BlockSpec(memory_space=pl.ANY)
    res = pl.pallas_call(
        wrapped, name=name, grid=grid, in_specs=list(in_specs) + [any_spec] * ci,
        out_specs=out_specs_l + [any_spec] * co, out_shape=out_shape_l + merged.out_shapes,
        scratch_shapes=scratch_shapes + merged.sems,
        compiler_params=_params(("arbitrary",) * len(grid)))(*args, *merged.inputs)
    outs, couts = res[:n_out], res[n_out:]
    return (outs[0] if single else list(outs)), [list(couts[a:b]) for a, b in spans]


def _comm_only(comm, name):
    def body():
        pass

    _, couts = _call(body, name=name, grid=(1,), in_specs=[], out_specs=[], out_shape=[], args=[], comm=comm)
    return couts[0]


def _tile(dim, pref):
    best = None
    for t in range(LANE, min(dim, pref) + 1, LANE):
        if dim % t == 0:
            best = t
    return best if best is not None else dim


MM_VMEM_BUDGET = 40 * 1024 * 1024


def _mm_tiles(m, n, k, a_bytes, b_bytes, o_bytes, r_bytes):
    for tk in (k, _tile(k, 2048), _tile(k, 1024), _tile(k, 512)):
        for tm, tn in ((1024, 1536), (1024, 1024), (1024, 512), (512, 512), (256, 512), (256, 256)):
            tm, tn = _tile(m, tm), _tile(n, tn)
            acc = 0 if tk == k else tm * tn * 4
            need = 2 * (tm * tk * a_bytes + tk * tn * b_bytes + tm * tn * (o_bytes + r_bytes)) + acc
            if need <= MM_VMEM_BUDGET:
                return tm, tn, tk
    raise ValueError("no matmul tiling fits VMEM")


def _mm(a, b, mode, name, out_dtype=F32, res=None, comm=None):
    if mode == "nn":
        (m, k), n = a.shape, b.shape[1]
    elif mode == "nt":
        (m, k), n = a.shape, b.shape[0]
    else:
        (k, m), n = a.shape, b.shape[1]
    tm, tn, tk = _mm_tiles(m, n, k, a.dtype.itemsize, b.dtype.itemsize, jnp.dtype(out_dtype).itemsize,
                           0 if res is None else res.dtype.itemsize)
    nk = k // tk
    dims = {"nn": ((1,), (0,)), "nt": ((1,), (1,)), "tn": ((0,), (0,))}[mode]

    def body(*refs):
        if res is None:
            a_ref, b_ref, o_ref = refs[:3]
            r_ref = None
        else:
            a_ref, b_ref, r_ref, o_ref = refs[:4]
        part = lax.dot_general(a_ref[...].astype(BF16), b_ref[...].astype(BF16), (dims, ((), ())),
                               preferred_element_type=F32)

        def finish(out):
            if r_ref is not None:
                out = out + r_ref[...].astype(F32)
            o_ref[...] = out.astype(out_dtype)

        if nk == 1:
            finish(part)
            return
        acc = refs[-1]
        kk = pl.program_id(2)

        @pl.when(kk == 0)
        def _():
            acc[...] = part

        @pl.when(kk > 0)
        def _():
            acc[...] += part

        @pl.when(kk == nk - 1)
        def _():
            finish(acc[...])

    a_spec = (pl.BlockSpec((tk, tm), lambda i, j, q: (q, i)) if mode == "tn"
              else pl.BlockSpec((tm, tk), lambda i, j, q: (i, q)))
    b_spec = (pl.BlockSpec((tn, tk), lambda i, j, q: (j, q)) if mode == "nt"
              else pl.BlockSpec((tk, tn), lambda i, j, q: (q, j)))
    o_spec = pl.BlockSpec((tm, tn), lambda i, j, q: (i, j))
    in_specs, args = [a_spec, b_spec], [a, b]
    if res is not None:
        in_specs.append(o_spec)
        args.append(res)
    return _call(body, name=name, grid=(m // tm, n // tn, nk), in_specs=in_specs, out_specs=o_spec,
                 out_shape=jax.ShapeDtypeStruct((m, n), out_dtype),
                 scratch_shapes=[] if nk == 1 else [pltpu.VMEM((tm, tn), F32)],
                 sem=("parallel", "parallel", "arbitrary"), args=args, comm=comm)


def _mm_bd(a, b, mode, name, out_dtype=F32, res=None, comm=None, tm=1024):
    if mode == "tn":
        k = a.shape[0]
        nb = min(a.shape[1], b.shape[1]) // LANE
        ma, n = a.shape[1] // nb, b.shape[1] // nb

        def body(a_ref, b_ref, o_ref):
            o_ref[0] = lax.dot_general(a_ref[...].astype(BF16), b_ref[...].astype(BF16), (((0,), (0,)), ((), ())),
                                       preferred_element_type=F32).astype(out_dtype)

        return _call(body, name=name, grid=(nb,),
                     in_specs=[pl.BlockSpec((k, ma), lambda j: (0, j)), pl.BlockSpec((k, n), lambda j: (0, j))],
                     out_specs=pl.BlockSpec((1, ma, n), lambda j: (j, 0, 0)),
                     out_shape=jax.ShapeDtypeStruct((nb, ma, n), out_dtype), sem=("parallel",), args=(a, b), comm=comm)
    m = a.shape[0]
    nb = b.shape[0]
    ka = a.shape[1] // nb
    n = b.shape[2] if mode == "nn" else b.shape[1]
    tm = _tile(m, tm)
    dims = ((1,), (0,)) if mode == "nn" else ((1,), (1,))

    def body(*refs):
        if res is None:
            a_ref, b_ref, o_ref = refs
            r_ref = None
        else:
            a_ref, b_ref, r_ref, o_ref = refs
        out = lax.dot_general(a_ref[...].astype(BF16), b_ref[0].astype(BF16), (dims, ((), ())),
                              preferred_element_type=F32)
        if r_ref is not None:
            out = out + r_ref[...].astype(F32)
        o_ref[...] = out.astype(out_dtype)

    o_spec = pl.BlockSpec((tm, n), lambda i, j: (i, j))
    in_specs = [pl.BlockSpec((tm, ka), lambda i, j: (i, j)), pl.BlockSpec((1,) + b.shape[1:], lambda i, j: (j, 0, 0))]
    args = [a, b]
    if res is not None:
        in_specs.append(o_spec)
        args.append(res)
    return _call(body, name=name, grid=(m // tm, nb), in_specs=in_specs, out_specs=o_spec,
                 out_shape=jax.ShapeDtypeStruct((m, nb * n), out_dtype), sem=("parallel", "parallel"),
                 args=args, comm=comm)


def _rms_fwd(x, g, out_dtype, name, tr=256):
    rows, d = x.shape

    def body(x_ref, g_ref, o_ref):
        xv = x_ref[...]
        r = lax.rsqrt(jnp.mean(xv * xv, axis=-1, keepdims=True) + RMS_EPS)
        o_ref[...] = (xv * r * g_ref[...]).astype(out_dtype)

    return pl.pallas_call(
        body, name=name, grid=(rows // tr,),
        in_specs=[pl.BlockSpec((tr, d), lambda i: (i, 0)), pl.BlockSpec((1, d), lambda i: (0, 0))],
        out_specs=pl.BlockSpec((tr, d), lambda i: (i, 0)), out_shape=jax.ShapeDtypeStruct((rows, d), out_dtype),
        compiler_params=_params(("parallel",)))(x, g.reshape(1, d))


def _rms_bwd(x, g, dy, dres, name, tr=256, comm=None):
    rows, d = x.shape

    def body(*refs):
        if dres is None:
            x_ref, g_ref, dy_ref, dx_ref, dg_ref = refs
            r_ref = None
        else:
            x_ref, g_ref, dy_ref, r_ref, dx_ref, dg_ref = refs

        @pl.when(pl.program_id(0) == 0)
        def _():
            dg_ref[...] = jnp.zeros_like(dg_ref)

        xv, dyv = x_ref[...], dy_ref[...].astype(F32)
        r = lax.rsqrt(jnp.mean(xv * xv, axis=-1, keepdims=True) + RMS_EPS)
        xh = xv * r
        dyg = dyv * g_ref[...]
        dx = r * (dyg - xh * jnp.mean(dyg * xh, axis=-1, keepdims=True))
        if r_ref is not None:
            dx = dx + r_ref[...]
        dx_ref[...] = dx
        dg_ref[...] += jnp.sum(dyv * xh, axis=0, keepdims=True)

    blk = pl.BlockSpec((tr, d), lambda i: (i, 0))
    vec = pl.BlockSpec((1, d), lambda i: (0, 0))
    in_specs, args = [blk, vec, blk], [x, g.reshape(1, d), dy]
    if dres is not None:
        in_specs.append(blk)
        args.append(dres)
    return _call(
        body, name=name, grid=(rows // tr,), in_specs=in_specs, out_specs=[blk, vec],
        out_shape=[jax.ShapeDtypeStruct((rows, d), F32), jax.ShapeDtypeStruct((1, d), F32)],
        sem=("arbitrary",), args=args, comm=comm)


def _loss_head(x, g, target, name, tr=256):
    rows, d = x.shape

    def body(x_ref, g_ref, t_ref, loss_ref, dx_ref, dg_ref):
        @pl.when(pl.program_id(0) == 0)
        def _():
            dg_ref[...] = jnp.zeros_like(dg_ref)
            loss_ref[...] = jnp.zeros_like(loss_ref)

        xv = x_ref[...]
        r = lax.rsqrt(jnp.mean(xv * xv, axis=-1, keepdims=True) + RMS_EPS)
        xh = xv * r
        err = xh * g_ref[...] - t_ref[...]
        loss_ref[...] += 0.5 * jnp.sum(jnp.mean(err * err, axis=-1, keepdims=True), keepdims=True)
        dyv = err * (1.0 / d)
        dyg = dyv * g_ref[...]
        dx_ref[...] = r * (dyg - xh * jnp.mean(dyg * xh, axis=-1, keepdims=True))
        dg_ref[...] += jnp.sum(dyv * xh, axis=0, keepdims=True)

    blk = pl.BlockSpec((tr, d), lambda i: (i, 0))
    vec = pl.BlockSpec((1, d), lambda i: (0, 0))
    return pl.pallas_call(
        body, name=name, grid=(rows // tr,), in_specs=[blk, vec, blk],
        out_specs=[pl.BlockSpec((1, 1), lambda i: (0, 0)), blk, vec],
        out_shape=[jax.ShapeDtypeStruct((1, 1), F32), jax.ShapeDtypeStruct((rows, d), F32),
                   jax.ShapeDtypeStruct((1, d), F32)],
        compiler_params=_params(("arbitrary",)))(x, g.reshape(1, d), target)


def _shift_down(x, s):
    rows = lax.broadcasted_iota(jnp.int32, x.shape, 0)
    return jnp.where(rows >= s, pltpu.roll(x, s, 0), 0.0)


def _shift_up(x, s):
    n = x.shape[0]
    rows = lax.broadcasted_iota(jnp.int32, x.shape, 0)
    return jnp.where(rows < n - s, pltpu.roll(x, n - s, 0), 0.0)


def _sigmoid(x):
    return 1.0 / (1.0 + jnp.exp(-x))


def _silu_and_grad(x):
    s = _sigmoid(x)
    return x * s, s * (1.0 + x * (1.0 - s))


_GELU_C0, _GELU_C1 = math.sqrt(2.0 / math.pi), 0.044715


def _gelu_and_grad(x):
    th = jnp.tanh(_GELU_C0 * (x + _GELU_C1 * x * x * x))
    y = 0.5 * x * (1.0 + th)
    dy = 0.5 * (1.0 + th) + 0.5 * x * (1.0 - th * th) * _GELU_C0 * (1.0 + 3.0 * _GELU_C1 * x * x)
    return y, dy


def _ffn_act_fwd(h, w, name, tc=256, comm=None):
    t = h.shape[0]
    nb = D_FF // tc

    def body(hg_ref, hv_ref, wg_ref, wv_ref, a_ref):
        def conv(x, wr):
            return wr[2:3, :] * x + wr[1:2, :] * _shift_down(x, 1) + wr[0:1, :] * _shift_down(x, 2)

        cg = conv(hg_ref[...], wg_ref[...])
        cv = conv(hv_ref[...], wv_ref[...])
        a_ref[...] = (cg * _sigmoid(cg) * cv).astype(BF16)

    return _call(
        body, name=name, grid=(nb,),
        in_specs=[pl.BlockSpec((t, tc), lambda j: (0, j)), pl.BlockSpec((t, tc), lambda j: (0, j + nb)),
                  pl.BlockSpec((CONV_FFN, tc), lambda j: (0, j)), pl.BlockSpec((CONV_FFN, tc), lambda j: (0, j + nb))],
        out_specs=pl.BlockSpec((t, tc), lambda j: (0, j)), out_shape=jax.ShapeDtypeStruct((t, D_FF), BF16),
        sem=("parallel",), args=(h, h, w, w), comm=comm)


def _ffn_act_bwd(h, w, da, name, tc=256, comm=None):
    t = h.shape[0]
    nb = D_FF // tc

    def body(hg_ref, hv_ref, wg_ref, wv_ref, da_ref, dhg_ref, dhv_ref, dwg_ref, dwv_ref):
        hg, hv, wg, wv = hg_ref[...], hv_ref[...], wg_ref[...], wv_ref[...]
        hg1, hg2, hv1, hv2 = _shift_down(hg, 1), _shift_down(hg, 2), _shift_down(hv, 1), _shift_down(hv, 2)
        cg = wg[2:3, :] * hg + wg[1:2, :] * hg1 + wg[0:1, :] * hg2
        cv = wv[2:3, :] * hv + wv[1:2, :] * hv1 + wv[0:1, :] * hv2
        sg, dsg = _silu_and_grad(cg)
        dav = da_ref[...].astype(F32)
        dcv = dav * sg
        dcg = dav * cv * dsg

        def conv_t(dc, wr):
            return wr[2:3, :] * dc + wr[1:2, :] * _shift_up(dc, 1) + wr[0:1, :] * _shift_up(dc, 2)

        dhg_ref[...] = conv_t(dcg, wg).astype(BF16)
        dhv_ref[...] = conv_t(dcv, wv).astype(BF16)
        dwg_ref[0:1, :] = jnp.sum(dcg * hg2, axis=0, keepdims=True)
        dwg_ref[1:2, :] = jnp.sum(dcg * hg1, axis=0, keepdims=True)
        dwg_ref[2:3, :] = jnp.sum(dcg * hg, axis=0, keepdims=True)
        dwv_ref[0:1, :] = jnp.sum(dcv * hv2, axis=0, keepdims=True)
        dwv_ref[1:2, :] = jnp.sum(dcv * hv1, axis=0, keepdims=True)
        dwv_ref[2:3, :] = jnp.sum(dcv * hv, axis=0, keepdims=True)

    big = lambda off: pl.BlockSpec((t, tc), lambda j: (0, j + off))
    small = lambda off: pl.BlockSpec((CONV_FFN, tc), lambda j: (0, j + off))
    res = _call(
        body, name=name, grid=(nb,),
        in_specs=[big(0), big(nb), small(0), small(nb), big(0)],
        out_specs=[big(0), big(0), small(0), small(0)],
        out_shape=[jax.ShapeDtypeStruct((t, D_FF), BF16), jax.ShapeDtypeStruct((t, D_FF), BF16),
                   jax.ShapeDtypeStruct((CONV_FFN, D_FF), F32), jax.ShapeDtypeStruct((CONV_FFN, D_FF), F32)],
        sem=("parallel",), args=(h, h, w, w, da), comm=comm)
    (dhg, dhv, dwg, dwv), couts = res if comm is not None else (res, None)
    out = (jnp.concatenate([dhg, dhv], axis=1), jnp.concatenate([dwg, dwv], axis=1))
    return out if comm is None else (out, couts)


def _attn_probs(q, k):
    s = lax.dot_general(q.astype(BF16), k.astype(BF16), (((1,), (1,)), ((), ())),
                        preferred_element_type=F32) * (HEAD_X ** -0.5)
    s = s - jnp.max(s, axis=-1, keepdims=True)
    p = jnp.exp(s)
    return p / jnp.sum(p, axis=-1, keepdims=True)


def _attn_fwd(q, kv, name, tq=512, comm=None):
    t = q.shape[0]

    def body(q_ref, k_ref, v_ref, o_ref):
        p = _attn_probs(q_ref[...], k_ref[...])
        o_ref[...] = jnp.dot(p.astype(BF16), v_ref[...].astype(BF16), preferred_element_type=F32).astype(BF16)

    return _call(
        body, name=name, grid=(N_HEADS_X, t // tq),
        in_specs=[pl.BlockSpec((tq, HEAD_X), lambda h, i: (i, h)),
                  pl.BlockSpec((MEM_LEN, HEAD_X), lambda h, i: (0, h)),
                  pl.BlockSpec((MEM_LEN, HEAD_X), lambda h, i: (0, h + N_HEADS_X))],
        out_specs=pl.BlockSpec((tq, HEAD_X), lambda h, i: (i, h)),
        out_shape=jax.ShapeDtypeStruct((t, N_HEADS_X * HEAD_X), BF16),
        sem=("parallel", "parallel"), args=(q, kv, kv), comm=comm)


def _attn_bwd(q, kv, do, name, tq=512):
    t = q.shape[0]

    def body(q_ref, k_ref, v_ref, do_ref, dq_ref, dk_ref, dv_ref):
        @pl.when(pl.program_id(1) == 0)
        def _():
            dk_ref[...] = jnp.zeros_like(dk_ref)
            dv_ref[...] = jnp.zeros_like(dv_ref)

        qb, kb, vb, dob = (r[...].astype(BF16) for r in (q_ref, k_ref, v_ref, do_ref))
        p = _attn_probs(qb, kb)
        dp = lax.dot_general(dob, vb, (((1,), (1,)), ((), ())), preferred_element_type=F32)
        ds = p * (dp - jnp.sum(dp * p, axis=-1, keepdims=True)) * (HEAD_X ** -0.5)
        dsb = ds.astype(BF16)
        dq_ref[...] = jnp.dot(dsb, kb, preferred_element_type=F32).astype(BF16)
        dk_ref[...] += lax.dot_general(dsb, qb, (((0,), (0,)), ((), ())), preferred_element_type=F32)
        dv_ref[...] += lax.dot_general(p.astype(BF16), dob, (((0,), (0,)), ((), ())), preferred_element_type=F32)

    qs = pl.BlockSpec((tq, HEAD_X), lambda h, i: (i, h))
    ms = pl.BlockSpec((MEM_LEN, HEAD_X), lambda h, i: (0, h))
    return pl.pallas_call(
        body, name=name, grid=(N_HEADS_X, t // tq),
        in_specs=[qs, ms, pl.BlockSpec((MEM_LEN, HEAD_X), lambda h, i: (0, h + N_HEADS_X)), qs],
        out_specs=[qs, ms, ms],
        out_shape=[jax.ShapeDtypeStruct((t, D_MODEL), BF16), jax.ShapeDtypeStruct((MEM_LEN, D_MODEL), F32),
                   jax.ShapeDtypeStruct((MEM_LEN, D_MODEL), F32)],
        compiler_params=_params(("parallel", "arbitrary")))(q, kv, kv, do)


def _pool_counts(t, win):
    pos = lax.broadcasted_iota(jnp.int32, (t, 1), 0).astype(F32) + 1.0
    return 1.0 / jnp.minimum(pos, float(win))


def _pool_delta(xv, win):
    s, step = xv, 1
    while step < win:
        s = s + _shift_down(s, step)
        step *= 2
    return s * _pool_counts(xv.shape[0], win) - xv


def _pool_delta_t(dv, win):
    s, step = dv * _pool_counts(dv.shape[0], win), 1
    while step < win:
        s = s + _shift_up(s, step)
        step *= 2
    return s - dv


def _pool_fwd(xn, w, scale, res, name):
    t = xn.shape[0]

    def make_branch(win, xn_ref, w_ref, s_ref, r_ref, o_ref):
        def branch():
            dl = _pool_delta(xn_ref[...], win)
            y = jnp.dot(dl.astype(BF16), w_ref[0], preferred_element_type=F32)
            o_ref[...] = r_ref[...] + y * s_ref[...]
        return branch

    def body(xn_ref, w_ref, s_ref, r_ref, o_ref):
        for gi, win in enumerate(POOL_WINDOWS):
            pl.when(pl.program_id(0) == gi)(make_branch(win, xn_ref, w_ref, s_ref, r_ref, o_ref))

    blk = pl.BlockSpec((t, POOL_GROUP), lambda g: (0, g))
    return pl.pallas_call(
        body, name=name, grid=(len(POOL_WINDOWS),),
        in_specs=[blk, pl.BlockSpec((1, POOL_GROUP, POOL_GROUP), lambda g: (g, 0, 0)),
                  pl.BlockSpec((1, POOL_GROUP), lambda g: (0, g)), blk],
        out_specs=blk, out_shape=jax.ShapeDtypeStruct((t, D_MODEL), F32),
        compiler_params=_params(("parallel",)))(xn, w, scale, res)


def _pool_bwd(xn, w, scale, dmix, name):
    t = xn.shape[0]

    def make_branch(win, xn_ref, w_ref, s_ref, d_ref, dxn_ref, dw_ref, ds_ref):
        def branch():
            dl = _pool_delta(xn_ref[...], win).astype(BF16)
            wv = w_ref[0]
            dm = d_ref[...]
            y = jnp.dot(dl, wv, preferred_element_type=F32)
            ds_ref[...] = jnp.sum(dm * y, axis=0, keepdims=True)
            dy = (dm * s_ref[...]).astype(BF16)
            dw_ref[0] = lax.dot_general(dl, dy, (((0,), (0,)), ((), ())), preferred_element_type=F32)
            ddl = lax.dot_general(dy, wv, (((1,), (1,)), ((), ())), preferred_element_type=F32)
            dxn_ref[...] = _pool_delta_t(ddl, win)
        return branch

    def body(*refs):
        for gi, win in enumerate(POOL_WINDOWS):
            pl.when(pl.program_id(0) == gi)(make_branch(win, *refs))

    blk = pl.BlockSpec((t, POOL_GROUP), lambda g: (0, g))
    wspec = pl.BlockSpec((1, POOL_GROUP, POOL_GROUP), lambda g: (g, 0, 0))
    vec = pl.BlockSpec((1, POOL_GROUP), lambda g: (0, g))
    return pl.pallas_call(
        body, name=name, grid=(len(POOL_WINDOWS),), in_specs=[blk, wspec, vec, blk], out_specs=[blk, wspec, vec],
        out_shape=[jax.ShapeDtypeStruct((t, D_MODEL), F32),
                   jax.ShapeDtypeStruct((len(POOL_WINDOWS), POOL_GROUP, POOL_GROUP), F32),
                   jax.ShapeDtypeStruct((1, D_MODEL), F32)],
        compiler_params=_params(("parallel",)))(xn, w, scale, dmix)


def _qkv_conv(h, wr):
    return (wr[3:4, :] * h + wr[2:3, :] * _shift_down(h, 1) + wr[1:2, :] * _shift_down(h, 2)
            + wr[0:1, :] * _shift_down(h, 3))


def _qkv_pre_fwd(h, w, col0, ncols, normalize, scale, name):
    t = h.shape[0]

    def body(h_ref, w_ref, o_ref):
        c = _qkv_conv(h_ref[...], w_ref[...])
        s = c * _sigmoid(c)
        if normalize:
            s = s * lax.rsqrt(jnp.sum(s * s, axis=-1, keepdims=True) + 1e-6) * scale
        o_ref[...] = s

    return pl.pallas_call(
        body, name=name, grid=(ncols,),
        in_specs=[pl.BlockSpec((t, HEAD_A), lambda j: (0, j + col0)), pl.BlockSpec((CONV_A, HEAD_A), lambda j: (0, j + col0))],
        out_specs=pl.BlockSpec((t, HEAD_A), lambda j: (0, j)), out_shape=jax.ShapeDtypeStruct((t, ncols * HEAD_A), F32),
        compiler_params=_params(("parallel",)))(h, w)


def _qkv_pre_bwd(h, w, dy, col0, ncols, normalize, scale, name):
    t = h.shape[0]

    def body(h_ref, w_ref, dy_ref, dh_ref, dw_ref):
        hv, wr, dyv = h_ref[...], w_ref[...], dy_ref[...]
        h1, h2, h3 = _shift_down(hv, 1), _shift_down(hv, 2), _shift_down(hv, 3)
        c = wr[3:4, :] * hv + wr[2:3, :] * h1 + wr[1:2, :] * h2 + wr[0:1, :] * h3
        s, dsilu = _silu_and_grad(c)
        if normalize:
            r = lax.rsqrt(jnp.sum(s * s, axis=-1, keepdims=True) + 1e-6)
            y = s * r
            dyv = dyv * scale
            ds = r * (dyv - y * jnp.sum(dyv * y, axis=-1, keepdims=True))
        else:
            ds = dyv
        dc = ds * dsilu
        dh = (wr[3:4, :] * dc + wr[2:3, :] * _shift_up(dc, 1) + wr[1:2, :] * _shift_up(dc, 2)
              + wr[0:1, :] * _shift_up(dc, 3))
        dh_ref[...] = dh.astype(BF16)
        dw_ref[0:1, :] = jnp.sum(dc * h3, axis=0, keepdims=True)
        dw_ref[1:2, :] = jnp.sum(dc * h2, axis=0, keepdims=True)
        dw_ref[2:3, :] = jnp.sum(dc * h1, axis=0, keepdims=True)
        dw_ref[3:4, :] = jnp.sum(dc * hv, axis=0, keepdims=True)

    return pl.pallas_call(
        body, name=name, grid=(ncols,),
        in_specs=[pl.BlockSpec((t, HEAD_A), lambda j: (0, j + col0)), pl.BlockSpec((CONV_A, HEAD_A), lambda j: (0, j + col0)),
                  pl.BlockSpec((t, HEAD_A), lambda j: (0, j))],
        out_specs=[pl.BlockSpec((t, HEAD_A), lambda j: (0, j)), pl.BlockSpec((CONV_A, HEAD_A), lambda j: (0, j))],
        out_shape=[jax.ShapeDtypeStruct((t, ncols * HEAD_A), BF16), jax.ShapeDtypeStruct((CONV_A, ncols * HEAD_A), F32)],
        compiler_params=_params(("parallel",)))(h, w, dy)


def _softplus(x):
    return jnp.maximum(x, 0.0) + jnp.log1p(jnp.exp(-jnp.abs(x)))


def _gates_fwd(ba, arow, brow, name):
    t = ba.shape[0]

    def body(x_ref, a_ref, b_ref, o_ref):
        xv = x_ref[...]
        lane = lax.broadcasted_iota(jnp.int32, xv.shape, 1)
        beta = _sigmoid(xv)
        g = -jnp.exp(a_ref[...]) * _softplus(xv + b_ref[...])
        o_ref[...] = jnp.where(lane < N_HEADS_A, beta, jnp.where(lane < 2 * N_HEADS_A, g, 0.0))

    return pl.pallas_call(body, name=name, out_shape=jax.ShapeDtypeStruct((t, LANE), F32),
                          compiler_params=_params())(ba, arow, brow)


def _gates_bwd(ba, arow, brow, dgb, name):
    t = ba.shape[0]

    def body(x_ref, a_ref, b_ref, d_ref, dx_ref, da_ref, db_ref):
        xv = x_ref[...]
        dv = d_ref[0] + d_ref[1] + d_ref[2] + d_ref[3]
        lane = lax.broadcasted_iota(jnp.int32, xv.shape, 1)
        beta = _sigmoid(xv)
        ea = jnp.exp(a_ref[...])
        z = xv + b_ref[...]
        dgv = jnp.where((lane >= N_HEADS_A) & (lane < 2 * N_HEADS_A), dv, 0.0) * (-ea)
        dz = dgv * _sigmoid(z)
        dx = jnp.where(lane < N_HEADS_A, dv * beta * (1.0 - beta), dz)
        dx_ref[...] = dx.astype(BF16)
        db_ref[...] = jnp.sum(dz, axis=0, keepdims=True)
        da_ref[...] = jnp.sum(dgv * _softplus(z), axis=0, keepdims=True)

    return pl.pallas_call(
        body, name=name,
        out_shape=[jax.ShapeDtypeStruct((t, LANE), BF16), jax.ShapeDtypeStruct((1, LANE), F32),
                   jax.ShapeDtypeStruct((1, LANE), F32)],
        compiler_params=_params())(ba, arow, brow, dgb)


def _dot(a, b, prec=None):
    if prec is None:
        return jnp.dot(a.astype(BF16), b.astype(BF16), preferred_element_type=F32)
    return jnp.dot(a, b, precision=prec, preferred_element_type=F32)


def _dot_nt(a, b, prec=None):
    if prec is None:
        a, b = a.astype(BF16), b.astype(BF16)
    return lax.dot_general(a, b, (((1,), (1,)), ((), ())), precision=prec, preferred_element_type=F32)


def _dot_tn(a, b, prec=None):
    if prec is None:
        a, b = a.astype(BF16), b.astype(BF16)
    return lax.dot_general(a, b, (((0,), (0,)), ((), ())), precision=prec, preferred_element_type=F32)


def _gdr_chunk_terms(k, beta, g):
    c = GDR_CHUNK
    row = lax.broadcasted_iota(jnp.int32, (c, c), 0)
    col = lax.broadcasted_iota(jnp.int32, (c, c), 1)
    causal, strict = row >= col, row > col
    gcum = _dot(causal.astype(F32), jnp.broadcast_to(g, (c, c)), HIGHEST)
    diff = gcum - gcum.T
    decay = jnp.where(causal, jnp.exp(jnp.where(causal, diff, 0.0)), 0.0)
    kb = k * beta
    kk = _dot_nt(kb, k)
    return row, col, causal, strict, gcum, decay, kb, kk


def _unit_lower_inverse(a):
    c = a.shape[0]
    eye = (lax.broadcasted_iota(jnp.int32, (c, c), 0) == lax.broadcasted_iota(jnp.int32, (c, c), 1)).astype(F32)
    p = -a
    inv = eye + p
    step = 1
    while 2 * step < c:
        p = _dot(p, p, HIGH)
        inv = inv + _dot(inv, p, HIGH)
        step *= 2
    return inv


def _head_gates(gates, head):
    lane = lax.broadcasted_iota(jnp.int32, gates.shape, 1)
    beta = jnp.sum(jnp.where(lane == head, gates, 0.0), axis=1, keepdims=True)
    g = jnp.sum(jnp.where(lane == head + N_HEADS_A, gates, 0.0), axis=1, keepdims=True)
    return beta, g


def _gdr_fwd(q, k, v, gates, name, comm=None):
    t = q.shape[0]
    c = GDR_CHUNK
    n = t // c

    hps = GDR_HEADS_PER_STEP

    def one_head(hh, q_ref, k_ref, v_ref, gb_ref, o_ref, tm_ref, s_ref, state):
        cols = slice(hh * HEAD_A, (hh + 1) * HEAD_A)
        qv, kv, vv = q_ref[:, cols], k_ref[:, cols], v_ref[:, cols]
        beta, g = _head_gates(gb_ref[...], pl.program_id(0) * hps + hh)
        row, col, causal, strict, gcum, decay, kb, kk = _gdr_chunk_terms(kv, beta, g)
        tm = _unit_lower_inverse(jnp.where(strict, kk * decay, 0.0))
        e = jnp.exp(gcum)
        u = _dot(tm, vv * beta, HIGH)
        w = _dot(tm, kb * e, HIGH)
        p = jnp.where(causal, _dot_nt(qv, kv) * decay, 0.0)
        s = state[hh]
        s_ref[hh, 0] = s
        tm_ref[hh, 0] = tm
        vn = u - _dot(w, s)
        o_ref[:, cols] = _dot(qv * e, s) + _dot(p, vn)
        glast = gcum[c - 1:c, :]
        state[hh] = s * jnp.exp(glast) + _dot_tn(kv * jnp.exp(glast - gcum), vn)

    def body(*refs):
        state = refs[-1]

        @pl.when(pl.program_id(1) == 0)
        def _():
            state[...] = jnp.zeros_like(state)

        for hh in range(hps):
            one_head(hh, *refs)

    blk = pl.BlockSpec((c, hps * HEAD_A), lambda h, i: (i, h))
    mat = pl.BlockSpec((hps, 1, c, c), lambda h, i: (h, i, 0, 0))
    return _call(
        body, name=name, grid=(N_HEADS_A // hps, n),
        in_specs=[blk, blk, blk, pl.BlockSpec((c, LANE), lambda h, i: (i, 0))],
        out_specs=[blk, mat, mat],
        out_shape=[jax.ShapeDtypeStruct((t, WIDTH_A), F32), jax.ShapeDtypeStruct((N_HEADS_A, n, c, c), F32),
                   jax.ShapeDtypeStruct((N_HEADS_A, n, HEAD_A, HEAD_A), F32)],
        scratch_shapes=[pltpu.VMEM((hps, HEAD_A, HEAD_A), F32)], sem=("parallel", "arbitrary"),
        args=(q, k, v, gates), comm=comm)


def _gdr_bwd(q, k, v, gates, tm_all, s_all, do, name, comm=None):
    t = q.shape[0]
    c = GDR_CHUNK
    n = t // c

    hps = GDR_HEADS_PER_STEP

    def one_head(hh, q_ref, k_ref, v_ref, gb_ref, tm_ref, s_ref, do_ref, dq_ref, dk_ref, dv_ref, dgb_ref, dstate):
        cols = slice(hh * HEAD_A, (hh + 1) * HEAD_A)
        qv, kv, vv, dov = q_ref[:, cols], k_ref[:, cols], v_ref[:, cols], do_ref[:, cols]
        head = pl.program_id(0) * hps + hh
        beta, g = _head_gates(gb_ref[...], head)
        tm, s, dsp = tm_ref[hh, 0], s_ref[hh, 0], dstate[hh]
        row, col, causal, strict, gcum, decay, kb, kk = _gdr_chunk_terms(kv, beta, g)
        e = jnp.exp(gcum)
        vb, kbe = vv * beta, kb * e
        u = _dot(tm, vb, HIGH)
        w = _dot(tm, kbe, HIGH)
        qk = _dot_nt(qv, kv)
        p = jnp.where(causal, qk * decay, 0.0)
        vn = u - _dot(w, s)
        glast = gcum[c - 1:c, :]
        el = jnp.exp(glast)
        f = jnp.exp(glast - gcum)
        kd = kv * f
        qe = qv * e

        dvn = _dot_tn(p, dov) + _dot(kd, dsp)
        dglast = el[:, 0:1] * jnp.sum(s * dsp, keepdims=True)
        dkd = _dot_nt(vn, dsp)
        dk = dkd * f
        df = jnp.sum(dkd * kv, axis=1, keepdims=True) * f[:, 0:1]
        dglast = dglast + jnp.sum(df, keepdims=True)
        dgc = -df
        dp = jnp.where(causal, _dot_nt(dov, vn), 0.0)
        dqe = _dot_nt(dov, s)
        dq = dqe * e
        de = jnp.sum(dqe * qv, axis=1, keepdims=True)
        dstate[hh] = dsp * el + _dot_tn(qe, dov) - _dot_tn(w, dvn)
        dw = -_dot_nt(dvn, s)
        dvb = _dot_tn(tm, dvn, HIGH)
        dkbe = _dot_tn(tm, dw, HIGH)
        da = -jnp.where(strict, _dot_nt(dvb, u) + _dot_nt(dkbe, w), 0.0)
        dkk = da * decay
        dqk = dp * decay
        dd = da * kk + dp * qk
        dq = dq + _dot(dqk, kv)
        dk = dk + _dot_tn(dqk, qv)
        dkb = _dot(dkk, kv) + dkbe * e
        dk = dk + _dot_tn(dkk, kb)
        de = de + jnp.sum(dkbe * kb, axis=1, keepdims=True)
        dk = dk + dkb * beta
        dbeta = jnp.sum(dkb * kv, axis=1, keepdims=True) + jnp.sum(dvb * vv, axis=1, keepdims=True)
        m = dd * decay
        dgc = dgc + jnp.sum(m, axis=1, keepdims=True) - jnp.sum(m.T, axis=1, keepdims=True)
        dgc = dgc + de * e[:, 0:1]
        dgc = dgc + jnp.where(row[:, 0:1] == c - 1, dglast, 0.0)
        dg = _dot((row <= col).astype(F32), jnp.broadcast_to(dgc, (c, c)), HIGHEST)
        dq_ref[:, cols] = dq
        dk_ref[:, cols] = dk
        dv_ref[:, cols] = dvb * beta
        lane = lax.broadcasted_iota(jnp.int32, (c, LANE), 1)
        dgb_ref[hh] = jnp.where(lane == head, dbeta, jnp.where(lane == head + N_HEADS_A, dg, 0.0))

    def body(*refs):
        dstate = refs[-1]

        @pl.when(pl.program_id(1) == 0)
        def _():
            dstate[...] = jnp.zeros_like(dstate)

        for hh in range(hps):
            one_head(hh, *refs)

    blk = pl.BlockSpec((c, hps * HEAD_A), lambda h, i: (n - 1 - i, h))
    mat = pl.BlockSpec((hps, 1, c, c), lambda h, i: (h, n - 1 - i, 0, 0))
    return _call(
        body, name=name, grid=(N_HEADS_A // hps, n),
        in_specs=[blk, blk, blk, pl.BlockSpec((c, LANE), lambda h, i: (n - 1 - i, 0)), mat, mat, blk],
        out_specs=[blk, blk, blk, pl.BlockSpec((hps, c, LANE), lambda h, i: (h, n - 1 - i, 0))],
        out_shape=[jax.ShapeDtypeStruct((t, WIDTH_A), F32)] * 3 + [jax.ShapeDtypeStruct((N_HEADS_A, t, LANE), F32)],
        scratch_shapes=[pltpu.VMEM((hps, HEAD_A, HEAD_A), F32)], sem=("parallel", "arbitrary"),
        args=(q, k, v, gates, tm_all, s_all, do), comm=comm)


_B_NN, _B_NT, _B_TN = ((2,), (1,)), ((2,), (2,)), ((1,), (1,))


def _bdot(a, b, dims=_B_NN, prec=None):
    if prec is None:
        a, b = a.astype(BF16), b.astype(BF16)
    return lax.dot_general(a, b, (dims, ((0,), (0,))), precision=prec, preferred_element_type=F32)


def _heads_of(ref):
    return jnp.stack([ref[:, h * HEAD_A:(h + 1) * HEAD_A] for h in range(N_HEADS_A)])


def _all_head_gates(gates):
    pairs = [_head_gates(gates, h) for h in range(N_HEADS_A)]
    return jnp.stack([b for b, _ in pairs]), jnp.stack([g for _, g in pairs])


def _gdr_terms(k, beta, g):
    h, c = k.shape[0], GDR_CHUNK
    row = lax.broadcasted_iota(jnp.int32, (c, c), 0)
    col = lax.broadcasted_iota(jnp.int32, (c, c), 1)
    causal, strict = row >= col, row > col
    lower = jnp.broadcast_to(causal.astype(F32), (h, c, c))
    gcum = _bdot(lower, jnp.broadcast_to(g, (h, c, c)), prec=HIGHEST)
    diff = gcum - jnp.swapaxes(gcum, 1, 2)
    decay = jnp.where(causal, jnp.exp(jnp.where(causal, diff, 0.0)), 0.0)
    kb = k * beta
    return row, col, causal, strict, gcum, decay, kb, _bdot(kb, k, _B_NT)


def _unit_lower_inverses(a):
    c = a.shape[1]
    eye = (lax.broadcasted_iota(jnp.int32, (c, c), 0) == lax.broadcasted_iota(jnp.int32, (c, c), 1)).astype(F32)
    p = -a
    inv = eye + p
    step = 1
    while 2 * step < c:
        p = _bdot(p, p, prec=HIGH)
        inv = inv + _bdot(inv, p, prec=HIGH)
        step *= 2
    return inv


def _gdr_fwd(q, k, v, gates, name, comm=None):
    t = q.shape[0]
    c, nh = GDR_CHUNK, N_HEADS_A
    n = t // c

    def body(q_ref, k_ref, v_ref, gb_ref, o_ref, tm_ref, s_ref, state):
        @pl.when(pl.program_id(0) == 0)
        def _():
            state[...] = jnp.zeros_like(state)

        qv, kv, vv = _heads_of(q_ref), _heads_of(k_ref), _heads_of(v_ref)
        beta, g = _all_head_gates(gb_ref[...])
        row, col, causal, strict, gcum, decay, kb, kk = _gdr_terms(kv, beta, g)
        tm = _unit_lower_inverses(jnp.where(strict, kk * decay, 0.0))
        e = jnp.exp(gcum)
        u = _bdot(tm, vv * beta, prec=HIGH)
        w = _bdot(tm, kb * e, prec=HIGH)
        p = jnp.where(causal, _bdot(qv, kv, _B_NT) * decay, 0.0)
        s = state[...]
        s_ref[:, 0] = s
        tm_ref[:, 0] = tm
        vn = u - _bdot(w, s)
        o = _bdot(qv * e, s) + _bdot(p, vn)
        for h in range(nh):
            o_ref[:, h * HEAD_A:(h + 1) * HEAD_A] = o[h]
        glast = gcum[:, c - 1:c, :]
        state[...] = s * jnp.exp(glast) + _bdot(kv * jnp.exp(glast - gcum), vn, _B_TN)

    blk = pl.BlockSpec((c, WIDTH_A), lambda i: (i, 0))
    mat = pl.BlockSpec((nh, 1, c, c), lambda i: (0, i, 0, 0))
    return _call(
        body, name=name, grid=(n,), in_specs=[blk, blk, blk, pl.BlockSpec((c, LANE), lambda i: (i, 0))],
        out_specs=[blk, mat, mat],
        out_shape=[jax.ShapeDtypeStruct((t, WIDTH_A), F32), jax.ShapeDtypeStruct((nh, n, c, c), F32),
                   jax.ShapeDtypeStruct((nh, n, HEAD_A, HEAD_A), F32)],
        scratch_shapes=[pltpu.VMEM((nh, HEAD_A, HEAD_A), F32)], sem=("arbitrary",),
        args=(q, k, v, gates), comm=comm)


def _gdr_bwd(q, k, v, gates, tm_all, s_all, do, name, comm=None):
    t = q.shape[0]
    c, nh = GDR_CHUNK, N_HEADS_A
    n = t // c

    def body(q_ref, k_ref, v_ref, gb_ref, tm_ref, s_ref, do_ref, dq_ref, dk_ref, dv_ref, dgb_ref, dstate):
        @pl.when(pl.program_id(0) == 0)
        def _():
            dstate[...] = jnp.zeros_like(dstate)

        qv, kv, vv, dov = _heads_of(q_ref), _heads_of(k_ref), _heads_of(v_ref), _heads_of(do_ref)
        beta, g = _all_head_gates(gb_ref[...])
        tm, s, dsp = tm_ref[:, 0], s_ref[:, 0], dstate[...]
        row, col, causal, strict, gcum, decay, kb, kk = _gdr_terms(kv, beta, g)
        rowsum = lambda x: jnp.sum(x, axis=2, keepdims=True)
        e = jnp.exp(gcum)
        vb, kbe = vv * beta, kb * e
        u = _bdot(tm, vb, prec=HIGH)
        w = _bdot(tm, kbe, prec=HIGH)
        qk = _bdot(qv, kv, _B_NT)
        p = jnp.where(causal, qk * decay, 0.0)
        vn = u - _bdot(w, s)
        glast = gcum[:, c - 1:c, :]
        el = jnp.exp(glast)
        f = jnp.exp(glast - gcum)
        kd = kv * f
        qe = qv * e

        dvn = _bdot(p, dov, _B_TN) + _bdot(kd, dsp)
        dglast = el[:, :, 0:1] * jnp.sum(s * dsp, axis=(1, 2), keepdims=True)
        dkd = _bdot(vn, dsp, _B_NT)
        dk = dkd * f
        df = rowsum(dkd * kv) * f[:, :, 0:1]
        dglast = dglast + jnp.sum(df, axis=1, keepdims=True)
        dgc = -df
        dp = jnp.where(causal, _bdot(dov, vn, _B_NT), 0.0)
        dqe = _bdot(dov, s, _B_NT)
        dq = dqe * e
        de = rowsum(dqe * qv)
        dstate[...] = dsp * el + _bdot(qe, dov, _B_TN) - _bdot(w, dvn, _B_TN)
        dw = -_bdot(dvn, s, _B_NT)
        dvb = _bdot(tm, dvn, _B_TN, prec=HIGH)
        dkbe = _bdot(tm, dw, _B_TN, prec=HIGH)
        da = -jnp.where(strict, _bdot(dvb, u, _B_NT) + _bdot(dkbe, w, _B_NT), 0.0)
        dkk = da * decay
        dqk = dp * decay
        dd = da * kk + dp * qk
        dq = dq + _bdot(dqk, kv)
        dk = dk + _bdot(dqk, qv, _B_TN)
        dkb = _bdot(dkk, kv) + dkbe * e
        dk = dk + _bdot(dkk, kb, _B_TN)
        de = de + rowsum(dkbe * kb)
        dk = dk + dkb * beta
        dbeta = rowsum(dkb * kv) + rowsum(dvb * vv)
        m = dd * decay
        dgc = dgc + rowsum(m) - rowsum(jnp.swapaxes(m, 1, 2))
        dgc = dgc + de * e[:, :, 0:1]
        dgc = dgc + jnp.where(row[:, 0:1] == c - 1, dglast, 0.0)
        upper = jnp.broadcast_to((row <= col).astype(F32), (nh, c, c))
        dg = _bdot(upper, jnp.broadcast_to(dgc, (nh, c, c)), prec=HIGHEST)
        dv = dvb * beta
        for h in range(nh):
            cols = slice(h * HEAD_A, (h + 1) * HEAD_A)
            dq_ref[:, cols] = dq[h]
            dk_ref[:, cols] = dk[h]
            dv_ref[:, cols] = dv[h]
        head = lax.broadcasted_iota(jnp.int32, (nh, c, LANE), 0)
        lane = lax.broadcasted_iota(jnp.int32, (nh, c, LANE), 2)
        dgb_ref[...] = jnp.where(lane == head, dbeta, jnp.where(lane == head + nh, dg, 0.0))

    blk = pl.BlockSpec((c, WIDTH_A), lambda i: (n - 1 - i, 0))
    mat = pl.BlockSpec((nh, 1, c, c), lambda i: (0, n - 1 - i, 0, 0))
    return _call(
        body, name=name, grid=(n,),
        in_specs=[blk, blk, blk, pl.BlockSpec((c, LANE), lambda i: (n - 1 - i, 0)), mat, mat, blk],
        out_specs=[blk, blk, blk, pl.BlockSpec((nh, c, LANE), lambda i: (0, n - 1 - i, 0))],
        out_shape=[jax.ShapeDtypeStruct((t, WIDTH_A), F32)] * 3 + [jax.ShapeDtypeStruct((nh, t, LANE), F32)],
        scratch_shapes=[pltpu.VMEM((nh, HEAD_A, HEAD_A), F32)], sem=("arbitrary",),
        args=(q, k, v, gates, tm_all, s_all, do), comm=comm)


def _onorm_fwd(o, gate, g, name):
    t = o.shape[0]

    def body(o_ref, gate_ref, g_ref, y_ref):
        ov, gv = o_ref[...], gate_ref[...]
        r = lax.rsqrt(jnp.mean(ov * ov, axis=-1, keepdims=True) + RMS_EPS)
        y_ref[...] = (ov * r * g_ref[...] * gv * _sigmoid(gv)).astype(BF16)

    blk = pl.BlockSpec((t, HEAD_A), lambda j: (0, j))
    return pl.pallas_call(
        body, name=name, grid=(N_HEADS_A,), in_specs=[blk, blk, pl.BlockSpec((1, HEAD_A), lambda j: (0, 0))],
        out_specs=blk, out_shape=jax.ShapeDtypeStruct((t, WIDTH_A), BF16),
        compiler_params=_params(("parallel",)))(o, gate, g)


def _onorm_bwd(o, gate, g, dy, name):
    t = o.shape[0]

    def body(o_ref, gate_ref, g_ref, dy_ref, do_ref, dgate_ref, dg_ref):
        @pl.when(pl.program_id(0) == 0)
        def _():
            dg_ref[...] = jnp.zeros_like(dg_ref)

        ov, gv, dyv = o_ref[...], gate_ref[...], dy_ref[...].astype(F32)
        r = lax.rsqrt(jnp.mean(ov * ov, axis=-1, keepdims=True) + RMS_EPS)
        oh = ov * r
        sg, dsg = _silu_and_grad(gv)
        dgate_ref[...] = (dyv * oh * g_ref[...] * dsg).astype(BF16)
        dn = dyv * sg
        dg_ref[...] += jnp.sum(dn * oh, axis=0, keepdims=True)
        dng = dn * g_ref[...]
        do_ref[...] = r * (dng - oh * jnp.mean(dng * oh, axis=-1, keepdims=True))

    blk = pl.BlockSpec((t, HEAD_A), lambda j: (0, j))
    vec = pl.BlockSpec((1, HEAD_A), lambda j: (0, 0))
    return pl.pallas_call(
        body, name=name, grid=(N_HEADS_A,), in_specs=[blk, blk, vec, blk], out_specs=[blk, blk, vec],
        out_shape=[jax.ShapeDtypeStruct((t, WIDTH_A), F32), jax.ShapeDtypeStruct((t, WIDTH_A), BF16),
                   jax.ShapeDtypeStruct((1, HEAD_A), F32)],
        compiler_params=_params(("arbitrary",)))(o, gate, g, dy)


def _cmul(ar, ai, br, bi):
    return ar * br - ai * bi, ar * bi + ai * br


def _scan_tables(ar, ai, reverse):
    p1 = (ar, ai)
    p2 = _cmul(*p1, *p1)
    p4 = _cmul(*p2, *p2)
    p8 = _cmul(*p4, *p4)
    p3 = _cmul(*p2, *p1)
    p5 = _cmul(*p4, *p1)
    p6 = _cmul(*p4, *p2)
    p7 = _cmul(*p4, *p3)
    pows = [p1, p2, p3, p4, p5, p6, p7, p8]
    rows = lax.broadcasted_iota(jnp.int32, (8, ar.shape[1]), 0)
    tr = jnp.zeros((8, ar.shape[1]), F32)
    ti = jnp.zeros((8, ar.shape[1]), F32)
    for r in range(8):
        pw = pows[7 - r] if reverse else pows[r]
        tr = jnp.where(rows == r, pw[0], tr)
        ti = jnp.where(rows == r, pw[1], ti)
    return p1, p2, p4, p8, tr, ti


def _tile_scan(xr, xi, p1, p2, p4, reverse):
    rows = lax.broadcasted_iota(jnp.int32, xr.shape, 0)
    for s, (pr, pi) in ((1, p1), (2, p2), (4, p4)):
        if reverse:
            keep = rows < 8 - s
            sr, si = pltpu.roll(xr, 8 - s, 0), pltpu.roll(xi, 8 - s, 0)
        else:
            keep = rows >= s
            sr, si = pltpu.roll(xr, s, 0), pltpu.roll(xi, s, 0)
        sr, si = jnp.where(keep, sr, 0.0), jnp.where(keep, si, 0.0)
        mr, mi = _cmul(pr, pi, sr, si)
        xr, xi = xr + mr, xi + mi
    return xr, xi


def _s5_scan_fwd(bu, a, name, tb=512, comm=None):
    t = bu.shape[0]
    cb = SCAN_CB
    nt = t // tb

    def body(b_ref, a_ref, x_ref, carry):
        @pl.when(pl.program_id(1) == 0)
        def _():
            carry[...] = jnp.zeros_like(carry)

        ar, ai = a_ref[:, 0:cb], a_ref[:, cb:2 * cb]
        p1, p2, p4, p8, tr, ti = _scan_tables(ar, ai, False)

        def step(j, c):
            cr, ci = c
            i = pl.multiple_of(j * 8, 8)
            xr, xi = _tile_scan(b_ref[pl.ds(i, 8), 0:cb], b_ref[pl.ds(i, 8), cb:2 * cb], p1, p2, p4, False)
            mr, mi = _cmul(tr, ti, cr, ci)
            xr, xi = xr + mr, xi + mi
            x_ref[pl.ds(i, 8), 0:cb] = xr
            x_ref[pl.ds(i, 8), cb:2 * cb] = xi
            return xr[7:8, :], xi[7:8, :]

        cr, ci = lax.fori_loop(0, tb // 8, step, (carry[0:1, :], carry[1:2, :]), unroll=2)
        carry[0:1, :] = cr
        carry[1:2, :] = ci

    blk = pl.BlockSpec((tb, 2 * cb), lambda j, i: (i, j))
    return _call(
        body, name=name, grid=(SSM_CH // cb, nt),
        in_specs=[blk, pl.BlockSpec((1, 2 * cb), lambda j, i: (0, j))], out_specs=blk,
        out_shape=jax.ShapeDtypeStruct((t, 2 * SSM_CH), F32), scratch_shapes=[pltpu.VMEM((8, cb), F32)],
        sem=("parallel", "arbitrary"), args=(bu, a), comm=comm)


def _s5_scan_bwd(dx, x, a, name, tb=512, comm=None):
    t = dx.shape[0]
    cb = SCAN_CB
    nt = t // tb
    nj = tb // 8

    def body(d_ref, x_ref, xp_ref, a_ref, l_ref, da_ref, carry, acc):
        tblk = pl.program_id(1)

        @pl.when(tblk == 0)
        def _():
            carry[...] = jnp.zeros_like(carry)
            acc[...] = jnp.zeros_like(acc)

        ar, ai = a_ref[:, 0:cb], a_ref[:, cb:2 * cb]
        p1, p2, p4, p8, tr, ti = _scan_tables(ar, -ai, True)
        rows = lax.broadcasted_iota(jnp.int32, (8, cb), 0)

        def step(jj, c):
            cr, ci, sr_acc, si_acc = c
            j = nj - 1 - jj
            i = pl.multiple_of(j * 8, 8)
            lr, li = _tile_scan(d_ref[pl.ds(i, 8), 0:cb], d_ref[pl.ds(i, 8), cb:2 * cb], p1, p2, p4, True)
            mr, mi = _cmul(tr, ti, cr, ci)
            lr, li = lr + mr, li + mi
            l_ref[pl.ds(i, 8), 0:cb] = lr
            l_ref[pl.ds(i, 8), cb:2 * cb] = li
            ip = pl.multiple_of(jnp.maximum(j - 1, 0) * 8, 8)
            prev_r = jnp.where(j > 0, x_ref[pl.ds(ip, 8), 0:cb], xp_ref[:, 0:cb])
            prev_i = jnp.where(j > 0, x_ref[pl.ds(ip, 8), cb:2 * cb], xp_ref[:, cb:2 * cb])
            edge = jnp.where(jnp.logical_and(j == 0, tblk == nt - 1), 0.0, 1.0)
            xs_r = jnp.where(rows == 0, pltpu.roll(prev_r, 1, 0) * edge, pltpu.roll(x_ref[pl.ds(i, 8), 0:cb], 1, 0))
            xs_i = jnp.where(rows == 0, pltpu.roll(prev_i, 1, 0) * edge, pltpu.roll(x_ref[pl.ds(i, 8), cb:2 * cb], 1, 0))
            sr_acc = sr_acc + lr * xs_r + li * xs_i
            si_acc = si_acc + li * xs_r - lr * xs_i
            return lr[0:1, :], li[0:1, :], sr_acc, si_acc

        cr, ci, sr_acc, si_acc = lax.fori_loop(
            0, nj, step, (carry[0:1, :], carry[1:2, :], acc[:, 0:cb], acc[:, cb:2 * cb]))
        carry[0:1, :] = cr
        carry[1:2, :] = ci
        acc[:, 0:cb] = sr_acc
        acc[:, cb:2 * cb] = si_acc

        @pl.when(tblk == nt - 1)
        def _():
            da_ref[...] = jnp.sum(acc[...], axis=0, keepdims=True)

    blk = pl.BlockSpec((tb, 2 * cb), lambda j, i: (nt - 1 - i, j))
    prev = pl.BlockSpec((8, 2 * cb), lambda j, i: (jnp.maximum((nt - 1 - i) * (tb // 8) - 1, 0), j))
    vec = pl.BlockSpec((1, 2 * cb), lambda j, i: (0, j))
    return _call(
        body, name=name, grid=(SSM_CH // cb, nt), in_specs=[blk, blk, prev, vec], out_specs=[blk, vec],
        out_shape=[jax.ShapeDtypeStruct((t, 2 * SSM_CH), F32), jax.ShapeDtypeStruct((1, 2 * SSM_CH), F32)],
        scratch_shapes=[pltpu.VMEM((8, cb), F32), pltpu.VMEM((8, 2 * cb), F32)],
        sem=("parallel", "arbitrary"), args=(dx, x, x, a), comm=comm)


def _glu_fwd(yc, u, dvec, wg, bg, name, tr=256):
    t = yc.shape[0]

    def body(yc_ref, u_ref, d_ref, w_ref, b_ref, yl_ref, yb_ref):
        yl = yc_ref[...] + d_ref[...] * u_ref[...]
        yl_ref[...] = yl
        yg, _ = _gelu_and_grad(yl)
        z = jnp.dot(yg.astype(BF16), w_ref[...], preferred_element_type=F32) + b_ref[...]
        yb_ref[...] = (yg * _sigmoid(z)).astype(BF16)

    blk = pl.BlockSpec((tr, SSM_WIDTH), lambda i: (i, 0))
    vec = pl.BlockSpec((1, SSM_WIDTH), lambda i: (0, 0))
    return pl.pallas_call(
        body, name=name, grid=(t // tr,),
        in_specs=[blk, blk, vec, pl.BlockSpec((SSM_WIDTH, SSM_WIDTH), lambda i: (0, 0)), vec],
        out_specs=[blk, blk],
        out_shape=[jax.ShapeDtypeStruct((t, SSM_WIDTH), F32), jax.ShapeDtypeStruct((t, SSM_WIDTH), BF16)],
        compiler_params=_params(("parallel",)))(yc, u, dvec, wg, bg)


def _glu_bwd(yl, u, dvec, wg, bg, dyb, name, tr=256):
    t = yl.shape[0]

    def body(yl_ref, u_ref, d_ref, w_ref, b_ref, dy_ref, dyl_ref, du_ref, dw_ref, db_ref, dd_ref):
        @pl.when(pl.program_id(0) == 0)
        def _():
            dw_ref[...] = jnp.zeros_like(dw_ref)
            db_ref[...] = jnp.zeros_like(db_ref)
            dd_ref[...] = jnp.zeros_like(dd_ref)

        ylv, dyv, wv = yl_ref[...], dy_ref[...].astype(F32), w_ref[...]
        yg, dgelu = _gelu_and_grad(ylv)
        ygb = yg.astype(BF16)
        z = jnp.dot(ygb, wv, preferred_element_type=F32) + b_ref[...]
        sg = _sigmoid(z)
        dz = dyv * yg * sg * (1.0 - sg)
        dzb = dz.astype(BF16)
        dyg = dyv * sg + lax.dot_general(dzb, wv, (((1,), (1,)), ((), ())), preferred_element_type=F32)
        dyl = dyg * dgelu
        dyl_ref[...] = dyl.astype(BF16)
        du_ref[...] = dyl * d_ref[...]
        dw_ref[...] += lax.dot_general(ygb, dzb, (((0,), (0,)), ((), ())), preferred_element_type=F32)
        db_ref[...] += jnp.sum(dz, axis=0, keepdims=True)
        dd_ref[...] += jnp.sum(dyl * u_ref[...], axis=0, keepdims=True)

    blk = pl.BlockSpec((tr, SSM_WIDTH), lambda i: (i, 0))
    vec = pl.BlockSpec((1, SSM_WIDTH), lambda i: (0, 0))
    wsp = pl.BlockSpec((SSM_WIDTH, SSM_WIDTH), lambda i: (0, 0))
    return pl.pallas_call(
        body, name=name, grid=(t // tr,), in_specs=[blk, blk, vec, wsp, vec, blk],
        out_specs=[blk, blk, wsp, vec, vec],
        out_shape=[jax.ShapeDtypeStruct((t, SSM_WIDTH), BF16), jax.ShapeDtypeStruct((t, SSM_WIDTH), F32),
                   jax.ShapeDtypeStruct((SSM_WIDTH, SSM_WIDTH), F32), jax.ShapeDtypeStruct((1, SSM_WIDTH), F32),
                   jax.ShapeDtypeStruct((1, SSM_WIDTH), F32)],
        compiler_params=_params(("arbitrary",)))(yl, u, dvec, wg, bg, dyb)


def _mesh_pos():
    return lax.axis_index("x"), lax.axis_index("y"), lax.axis_index("c")


def _device_index():
    x, y, c = _mesh_pos()
    return 4 * x + 2 * y + c


def _gather_comm(arrays):
    na = len(arrays)

    def own_copy(ins, outs, sems, ai):
        return pltpu.make_async_copy(ins[ai], outs[ai].at[_device_index()], sems[2].at[ai])

    def ctx(ins, outs, sems):
        send_sems, recv_sems = sems[:2]
        x, y, c = _mesh_pos()
        chips = [(1 - x, y), (x, 1 - y), (1 - x, 1 - y)]

        def copy(ai, kk, block, to, own=False):
            slot = outs[ai].at[4 * block[0] + 2 * block[1] + block[2]]
            return pltpu.make_async_remote_copy(
                src_ref=ins[ai] if own else slot, dst_ref=slot, send_sem=send_sems.at[ai, kk],
                recv_sem=recv_sems.at[ai, kk], device_id=to, device_id_type=MESH)

        return (x, y, c), (x, y, 1 - c), chips, c, copy

    def start(ins, outs, sems):
        me, sibling, chips, c, copy = ctx(ins, outs, sems)
        for ai in range(na):
            copy(ai, 0, me, sibling, own=True).start()
            for j, chip in enumerate(chips):
                copy(ai, 1 + j, me, (*chip, c), own=True).start()
        for ai in range(na):
            own_copy(ins, outs, sems, ai).start()

    def mid(ins, outs, sems):
        me, sibling, chips, c, copy = ctx(ins, outs, sems)
        for ai in range(na):
            for j, chip in enumerate(chips):
                copy(ai, 1 + j, (*chip, c), me).wait_recv()
                copy(ai, 4 + j, (*chip, c), sibling).start()

    def end(ins, outs, sems):
        me, sibling, chips, c, copy = ctx(ins, outs, sems)
        for ai in range(na):
            copy(ai, 0, sibling, me).wait_recv()
            copy(ai, 0, me, sibling, own=True).wait_send()
            for j, chip in enumerate(chips):
                copy(ai, 4 + j, (*chip, 1 - c), me).wait_recv()
                copy(ai, 1 + j, me, (*chip, c), own=True).wait_send()
                copy(ai, 4 + j, (*chip, c), sibling).wait_send()
            own_copy(ins, outs, sems, ai).wait()

    return Comm(arrays, [jax.ShapeDtypeStruct((N_DEV,) + a.shape, a.dtype) for a in arrays],
                [pltpu.SemaphoreType.DMA((na, 7)), pltpu.SemaphoreType.DMA((na, 7)), pltpu.SemaphoreType.DMA((na,))],
                start, end, mid)


def _swap_comm(arrays):
    na = len(arrays)
    offs = np.concatenate([[0], np.cumsum([a.shape[1] for a in arrays])]).astype(int)

    def copies(ins, outs, sems):
        x, y, c = _mesh_pos()
        return [pltpu.make_async_remote_copy(
            src_ref=ins[ai].at[2 * k + 1 - c], dst_ref=outs[0].at[k, pl.ds(int(offs[ai]), arrays[ai].shape[1])],
            send_sem=sems[0].at[ai, k], recv_sem=sems[1].at[ai, k], device_id=(x, y, 1 - c), device_id_type=MESH)
            for ai in range(na) for k in range(4)]

    def start(ins, outs, sems):
        for cp in copies(ins, outs, sems):
            cp.start()

    def end(ins, outs, sems):
        for cp in copies(ins, outs, sems):
            cp.wait()

    return Comm(arrays, [jax.ShapeDtypeStruct((4, int(offs[-1]), PACK_COLS), arrays[0].dtype)],
                [pltpu.SemaphoreType.DMA((na, 4)), pltpu.SemaphoreType.DMA((na, 4))], start, end)


def _chips_comm(send):
    def copies(ins, outs, sems):
        x, y, c = _mesh_pos()
        chips = [(1 - x, y), (x, 1 - y), (1 - x, 1 - y)]
        return [pltpu.make_async_remote_copy(
            src_ref=ins[0].at[2 * cx + cy], dst_ref=outs[0].at[j], send_sem=sems[0].at[j], recv_sem=sems[1].at[j],
            device_id=(cx, cy, c), device_id_type=MESH) for j, (cx, cy) in enumerate(chips)]

    def start(ins, outs, sems):
        for cp in copies(ins, outs, sems):
            cp.start()

    def end(ins, outs, sems):
        for cp in copies(ins, outs, sems):
            cp.wait()

    return Comm([send], [jax.ShapeDtypeStruct((3,) + send.shape[1:], send.dtype)],
                [pltpu.SemaphoreType.DMA((3,)), pltpu.SemaphoreType.DMA((3,))], start, end)


def _all_gather(arrays, name):
    na = len(arrays)

    def body(*refs):
        ins, outs = refs[:na], refs[na:2 * na]
        send_sems, recv_sems, local_sems = refs[2 * na:]
        x, y, c = _mesh_pos()
        me, sibling = (x, y, c), (x, y, 1 - c)
        chips = [(1 - x, y), (x, 1 - y), (1 - x, 1 - y)]
        waits = []
        for ai in range(na):
            in_ref, out_ref = ins[ai], outs[ai]

            def slot(px, py, pc, out_ref=out_ref):
                return out_ref.at[4 * px + 2 * py + pc]

            def copy(kk, block, to, src=None, ai=ai, slot=slot):
                return pltpu.make_async_remote_copy(
                    src_ref=slot(*block) if src is None else src, dst_ref=slot(*block),
                    send_sem=send_sems.at[ai, kk], recv_sem=recv_sems.at[ai, kk], device_id=to, device_id_type=MESH)

            mine = pltpu.make_async_copy(in_ref, slot(*me), local_sems.at[ai])
            mine.start()
            first = [copy(0, me, sibling, src=in_ref)]
            first += [copy(1 + j, me, (*chip, c), src=in_ref) for j, chip in enumerate(chips)]
            for cp in first:
                cp.start()
            waits.append((copy, mine, first))
        sends = []
        for ai in range(na):
            copy, mine, first = waits[ai]
            passed = [copy(4 + j, (*chip, c), sibling) for j, chip in enumerate(chips)]
            for j, chip in enumerate(chips):
                copy(1 + j, (*chip, c), me).wait_recv()
                passed[j].start()
            sends.append(passed)
        for ai in range(na):
            copy, mine, first = waits[ai]
            copy(0, sibling, me).wait_recv()
            for j, chip in enumerate(chips):
                copy(4 + j, (*chip, 1 - c), me).wait_recv()
            for cp in first + sends[ai]:
                cp.wait_send()
            mine.wait()

    any_spec = pl.BlockSpec(memory_space=pl.ANY)
    return pl.pallas_call(
        body, name=name, in_specs=[any_spec] * na, out_specs=[any_spec] * na,
        out_shape=[jax.ShapeDtypeStruct((N_DEV,) + a.shape, a.dtype) for a in arrays],
        scratch_shapes=[pltpu.SemaphoreType.DMA((na, 7)), pltpu.SemaphoreType.DMA((na, 7)),
                        pltpu.SemaphoreType.DMA((na,))],
        compiler_params=pltpu.CompilerParams(has_side_effects=True))(*arrays)


def _swap_sibling(arrays, name):
    na = len(arrays)
    offs = np.concatenate([[0], np.cumsum([a.shape[1] for a in arrays])]).astype(int)
    rows = int(offs[-1])

    def body(*refs):
        ins, recv_ref = refs[:na], refs[na]
        send_sems, recv_sems = refs[na + 1:]
        x, y, c = _mesh_pos()
        started = []
        for ai in range(na):
            span = pl.ds(int(offs[ai]), arrays[ai].shape[1])
            for k in range(4):
                remote = pltpu.make_async_remote_copy(
                    src_ref=ins[ai].at[2 * k + 1 - c], dst_ref=recv_ref.at[k, span], send_sem=send_sems.at[ai, k],
                    recv_sem=recv_sems.at[ai, k], device_id=(x, y, 1 - c), device_id_type=MESH)
                remote.start()
                started.append(remote)
        for remote in started:
            remote.wait()

    any_spec = pl.BlockSpec(memory_space=pl.ANY)
    return pl.pallas_call(
        body, name=name, in_specs=[any_spec] * na, out_specs=any_spec,
        out_shape=jax.ShapeDtypeStruct((4, rows, PACK_COLS), arrays[0].dtype),
        scratch_shapes=[pltpu.SemaphoreType.DMA((na, 4)), pltpu.SemaphoreType.DMA((na, 4))])(*arrays)


def _exchange_chips(send, name):
    def body(s_ref, o_ref, send_sems, recv_sems):
        x, y, c = _mesh_pos()
        chips = [(1 - x, y), (x, 1 - y), (1 - x, 1 - y)]
        cps = [pltpu.make_async_remote_copy(
            src_ref=s_ref.at[2 * cx + cy], dst_ref=o_ref.at[j], send_sem=send_sems.at[j], recv_sem=recv_sems.at[j],
            device_id=(cx, cy, c), device_id_type=MESH) for j, (cx, cy) in enumerate(chips)]
        for cp in cps:
            cp.start()
        for cp in cps:
            cp.wait()

    any_spec = pl.BlockSpec(memory_space=pl.ANY)
    return pl.pallas_call(
        body, name=name, in_specs=[any_spec], out_specs=any_spec,
        out_shape=jax.ShapeDtypeStruct((3,) + send.shape[1:], send.dtype),
        scratch_shapes=[pltpu.SemaphoreType.DMA((3,)), pltpu.SemaphoreType.DMA((3,))])(send)


def _pair_sum(keep, recv, name, tr=464):
    nchip, rows, cols = keep.shape

    def body(g_ref, r_ref, o_ref):
        o_ref[...] = (g_ref[...].astype(F32) + r_ref[...].astype(F32)).astype(BF16)

    blk = pl.BlockSpec((1, tr, cols), lambda k, i: (k, i, 0))
    return pl.pallas_call(
        body, name=name, grid=(nchip, rows // tr), in_specs=[blk, blk], out_specs=blk,
        out_shape=jax.ShapeDtypeStruct((nchip, rows, cols), BF16),
        compiler_params=_params(("parallel", "parallel")))(keep, recv)


def _chip_sum(own, others, name, tr=464):
    _, rows, cols = own.shape
    chip = (2 * lax.axis_index("x") + lax.axis_index("y")).astype(jnp.int32).reshape(1)

    def body(chip_ref, own_ref, oth_ref, o_ref):
        del chip_ref
        acc = own_ref[0].astype(F32)
        for j in range(3):
            acc = acc + oth_ref[j].astype(F32)
        o_ref[...] = acc

    grid_spec = pltpu.PrefetchScalarGridSpec(
        num_scalar_prefetch=1, grid=(rows // tr,),
        in_specs=[pl.BlockSpec((1, tr, cols), lambda i, chip_ref: (chip_ref[0], i, 0)),
                  pl.BlockSpec((3, tr, cols), lambda i, chip_ref: (0, i, 0))],
        out_specs=pl.BlockSpec((tr, cols), lambda i, chip_ref: (i, 0)))
    return pl.pallas_call(
        body, name=name, grid_spec=grid_spec, out_shape=jax.ShapeDtypeStruct((rows, cols), F32),
        compiler_params=_params(("parallel",)))(chip, own, others)


def _sum_leading(parts, name, tr=464):
    nparts, rows, cols = parts.shape
    tr = tr if rows % tr == 0 else rows

    def body(p_ref, o_ref):
        acc = p_ref[0].astype(F32)
        for i in range(1, nparts):
            acc = acc + p_ref[i].astype(F32)
        o_ref[...] = acc

    return pl.pallas_call(
        body, name=name, grid=(rows // tr,),
        in_specs=[pl.BlockSpec((nparts, tr, cols), lambda i: (0, i, 0))],
        out_specs=pl.BlockSpec((tr, cols), lambda i: (i, 0)), out_shape=jax.ShapeDtypeStruct((rows, cols), F32),
        compiler_params=_params(("parallel",)))(parts)


def _adamw(w, g, m, v, name, comm=None):
    shape = w.shape
    cols = shape[-1]
    lead = shape[0] if len(shape) >= 3 else 1
    rows = int(np.prod(shape[:-1])) // lead if len(shape) > 1 else 1
    w2, g2, m2, v2 = (a.reshape(lead, rows, cols) for a in (w, g, m, v))
    tr = rows
    for cand in (512, 256, 128, 64, 32, 16, 8):
        if rows % cand == 0 and rows > cand:
            tr = cand
            break
    bc1, bc2 = 1.0 - ADAM_B1 ** ADAM_STEP, 1.0 - ADAM_B2 ** ADAM_STEP

    def body(w_ref, g_ref, m_ref, v_ref, d_ref, nm_ref, nv_ref):
        gv = g_ref[...]
        nm = ADAM_B1 * m_ref[...] + (1.0 - ADAM_B1) * gv
        nv = ADAM_B2 * v_ref[...] + (1.0 - ADAM_B2) * (gv * gv)
        nm_ref[...] = nm
        nv_ref[...] = nv
        d_ref[...] = -ADAM_LR * ((nm / bc1) / (jnp.sqrt(nv / bc2) + ADAM_EPS) + ADAM_WD * w_ref[...])

    blk = pl.BlockSpec((1, tr, cols), lambda l, i: (l, i, 0))
    res = _call(body, name=name, grid=(lead, rows // tr), in_specs=[blk] * 4, out_specs=[blk] * 3,
                out_shape=[jax.ShapeDtypeStruct((lead, rows, cols), F32)] * 3, sem=("parallel", "parallel"),
                args=(w2, g2, m2, v2), comm=comm)
    outs, couts = res if comm is not None else (res, None)
    outs = tuple(o.reshape(shape) for o in outs)
    return outs if comm is None else (outs, couts)


WEIGHT_NAMES = ['norm_mix_g', 'norm_xa_g', 'norm_ffn_g', 'norm_mem_g', 'norm_final_g', 'w_in_ab', 'conv_qkv_a',
                'a_log_a', 'dt_bias_a', 'onorm_g_a', 'ssm_lambda_re', 'ssm_lambda_im', 'ssm_b_re', 'ssm_b_im',
                'ssm_c_re', 'ssm_c_im', 'ssm_d', 'ssm_log_dt', 'w_glu_b', 'b_glu_b', 'w_out_ab', 'pool_w',
                'pool_scale', 'xa_wq', 'xa_wkv', 'xa_wo', 'ffn_w_up', 'ffn_conv', 'ffn_w_down']
BIG_SHARDED = {'w_in_ab': ((1, 1024, 2568), 2), 'w_glu_b': ((1, 512, 512), 1), 'w_out_ab': ((1, 1024, 1024), 1),
               'pool_w': ((1, 4, 256, 256), 2), 'xa_wq': ((2, 1024, 1024), 1), 'xa_wkv': ((2, 1024, 2048), 2),
               'xa_wo': ((2, 1024, 1024), 1), 'ffn_w_up': ((2, 1024, 5632), 2), 'ffn_w_down': ((2, 2816, 1024), 1)}
SMALL_SHARDED = {'conv_qkv_a': ((1, 4, 1536), 2), 'pool_scale': ((1, 1024), 1), 'ffn_conv': ((2, 3, 5632), 2)}
REPLICATED = {'norm_mix_g': (2, 1024), 'norm_xa_g': (2, 1024), 'norm_ffn_g': (2, 1024), 'norm_mem_g': (1024,),
              'norm_final_g': (1024,), 'a_log_a': (1, 4), 'dt_bias_a': (1, 4), 'onorm_g_a': (1, 128),
              'ssm_lambda_re': (1, 32, 64), 'ssm_lambda_im': (1, 32, 64), 'ssm_b_re': (1, 32, 64, 16),
              'ssm_b_im': (1, 32, 64, 16), 'ssm_c_re': (1, 32, 16, 64), 'ssm_c_im': (1, 32, 16, 64),
              'ssm_d': (1, 32, 16), 'ssm_log_dt': (1, 32), 'b_glu_b': (1, 512)}
PACK_ROW_ALIGN = 8


def _shard_shape(shape, axis):
    return tuple(s // N_DEV if i == axis else s for i, s in enumerate(shape))


def _round_up(n, m):
    return (n + m - 1) // m * m


def _pack(arrays):
    total = sum(int(np.prod(a.shape)) for a in arrays)
    padded = _round_up(total, PACK_COLS * PACK_ROW_ALIGN)
    parts = [a.astype(F32).reshape(-1) for a in arrays]
    if padded != total:
        parts.append(jnp.zeros((padded - total,), F32))
    return jnp.concatenate(parts).reshape(padded // PACK_COLS, PACK_COLS)


def _unpack(packed, shapes):
    flat, out, off = packed.reshape(-1), [], 0
    for shape in shapes:
        size = int(np.prod(shape))
        out.append(flat[off:off + size].reshape(shape))
        off += size
    return out


def _split_shards(full, axis):
    shape = full.shape
    s = shape[axis] // N_DEV
    a = full.reshape(shape[:axis] + (N_DEV, s) + shape[axis + 1:])
    return jnp.moveaxis(a, axis, 0).reshape(N_DEV, -1)


def _merge_shards(pieces, shape, axis):
    sh = _shard_shape(shape, axis)
    a = pieces.reshape((N_DEV,) + sh)
    a = jnp.moveaxis(a, 0, axis)
    return a.reshape(shape)


_SCAN_NB = SSM_CH // SCAN_CB


def _to_scan_layout(m, axis):
    shape = m.shape
    m = m.reshape(shape[:axis] + (2, _SCAN_NB, SCAN_CB) + shape[axis + 1:])
    return jnp.swapaxes(m, axis, axis + 1).reshape(shape)


def _from_scan_layout(m, axis):
    shape = m.shape
    m = m.reshape(shape[:axis] + (_SCAN_NB, 2, SCAN_CB) + shape[axis + 1:])
    return jnp.swapaxes(m, axis, axis + 1).reshape(shape)


def _s5_discretise(lam_re, lam_im, b_re, b_im, log_dt):
    dt = jnp.exp(log_dt)[:, None]
    mag = jnp.exp(lam_re * dt)
    ang = lam_im * dt
    lb_re, lb_im = mag * jnp.cos(ang), mag * jnp.sin(ang)
    den = lam_re * lam_re + lam_im * lam_im
    nr, ni = lb_re - 1.0, lb_im
    coef_re = (nr * lam_re + ni * lam_im) / den
    coef_im = (ni * lam_re - nr * lam_im) / den
    bb_re = coef_re[..., None] * b_re - coef_im[..., None] * b_im
    bb_im = coef_re[..., None] * b_im + coef_im[..., None] * b_re
    return lb_re, lb_im, bb_re, bb_im


_GROUPS_PER_BLOCK = N_GROUPS // _SCAN_NB
_U_BLOCK = _GROUPS_PER_BLOCK * SSM_GROUP


def _s5_matrices(lb_re, lb_im, bb_re, bb_im, c_re, c_im):
    eye = jnp.eye(_GROUPS_PER_BLOCK, dtype=F32)
    blocked = lambda m: m.reshape((_SCAN_NB, _GROUPS_PER_BLOCK) + m.shape[1:])
    bmat = lambda bb: jnp.einsum('jgph,gk->jghkp', blocked(bb), eye).reshape(_SCAN_NB, _U_BLOCK, SCAN_CB)
    cmat = lambda cc: jnp.einsum('jghp,gk->jkpgh', blocked(cc), eye).reshape(_SCAN_NB, SCAN_CB, _U_BLOCK)
    b_in = jnp.concatenate([bmat(bb_re), bmat(bb_im)], axis=2)
    c_out = jnp.concatenate([cmat(c_re), -cmat(c_im)], axis=1)
    a_row = _to_scan_layout(jnp.concatenate([lb_re.reshape(1, SSM_CH), lb_im.reshape(1, SSM_CH)], axis=1), 1)
    return b_in, c_out, a_row


def _s5_matrix_grads(db_in, dc_out, da_row):
    da_nat = _from_scan_layout(da_row, 1)
    eye = jnp.eye(_GROUPS_PER_BLOCK, dtype=F32)
    nb, gb = _SCAN_NB, _GROUPS_PER_BLOCK
    bgrad = lambda m: jnp.einsum('jghkp,gk->jgph', m.reshape(nb, gb, SSM_GROUP, gb, SSM_STATE), eye
                                 ).reshape(N_GROUPS, SSM_STATE, SSM_GROUP)
    cgrad = lambda m: jnp.einsum('jkpgh,gk->jghp', m.reshape(nb, gb, SSM_STATE, gb, SSM_GROUP), eye
                                 ).reshape(N_GROUPS, SSM_GROUP, SSM_STATE)
    dbb_re, dbb_im = bgrad(db_in[:, :, :SCAN_CB]), bgrad(db_in[:, :, SCAN_CB:])
    dc_re, dc_im = cgrad(dc_out[:, :SCAN_CB]), -cgrad(dc_out[:, SCAN_CB:])
    dlb_re = da_nat[0, :SSM_CH].reshape(N_GROUPS, SSM_STATE)
    dlb_im = da_nat[0, SSM_CH:].reshape(N_GROUPS, SSM_STATE)
    return dlb_re, dlb_im, dbb_re, dbb_im, dc_re, dc_im


def _as_pieces(a):
    return a.reshape(N_DEV, a.shape[0] // N_DEV, a.shape[1])


def _hybrid_fwd(xn, x, wts, p, weights, riders):
    sv = {}
    hq = _mm(xn, wts['w_qkv_t'], "nt", "l0_in_qkv")
    gate = _mm(xn, wts['w_gate_t'], "nt", "l0_in_gate")
    ba = _mm(xn, wts['w_ba_t'], "nt", "l0_in_ba")
    u = _mm(xn, wts['w_u_t'], "nt", "l0_in_u")
    conv = p['conv_qkv']
    q = _qkv_pre_fwd(hq, conv, 0, 4, True, HEAD_A ** -0.5, "l0_q_pre")
    k = _qkv_pre_fwd(hq, conv, 4, 4, True, 1.0, "l0_k_pre")
    v = _qkv_pre_fwd(hq, conv, 8, 4, False, 1.0, "l0_v_pre")
    gates = _gates_fwd(ba, p['arow'], p['brow'], "l0_gates")
    o, tm_all, s_all = riders.run("l0_gdr_fwd", _gdr_fwd, q, k, v, gates)
    wts['w_glu'], wts['w_out'] = weights.full['w_glu'], weights.full['w_out']
    y_a = _onorm_fwd(o, gate, p['onorm_g'], "l0_onorm")
    bu = riders.run("l0_s5_bu", _mm_bd, u, p['b_in'], "nn")
    xs = riders.run("l0_s5_scan", _s5_scan_fwd, bu, p['a_row'])
    yc = riders.run("l0_s5_cx", _mm_bd, xs, p['c_out'], "nn")
    yl, y_b = _glu_fwd(yc, u, p['d_row'], wts['w_glu'], p['b_glu'], "l0_glu")
    mixed = jnp.concatenate([y_a, y_b], axis=1)
    x1 = _mm(mixed, wts['w_out'], "nn", "l0_out", res=x)
    sv.update(hq=hq, gate=gate, ba=ba, u=u, q=q, k=k, v=v, gb=gates, o=o, tm=tm_all, s=s_all, xs=xs, yl=yl, mixed=mixed)
    return x1, sv


def _hybrid_bwd(dx1, xn, wts, p, sv, riders):
    gr = {}
    dmixed = _mm(dx1, wts['w_out'], "nt", "l0_out_dx", out_dtype=BF16)
    riders.grad('w_out', _as_pieces(_mm(sv['mixed'], dx1, "tn", "l0_out_dw", out_dtype=BF16)))
    dya, dyb = dmixed[:, :WIDTH_A], dmixed[:, WIDTH_A:]
    dyl, du_direct, dw_glu, gr['b_glu_b'], dd = _glu_bwd(
        sv['yl'], sv['u'], p['d_row'], wts['w_glu'], p['b_glu'], dyb, "l0_glu_bwd")
    riders.grad('w_glu', dw_glu.astype(BF16).reshape(N_DEV, -1, PACK_COLS))
    dxs = riders.run("l0_s5_cx_dx", _mm_bd, dyl, p['c_out'], "nt")
    dc_out = _mm_bd(sv['xs'], dyl, "tn", "l0_s5_cx_dw")
    lam, da_row = riders.run("l0_s5_scan_bwd", _s5_scan_bwd, dxs, sv['xs'], p['a_row'])
    du = _mm_bd(lam, p['b_in'], "nt", "l0_s5_bu_dx", res=du_direct, out_dtype=BF16)
    db_in = _mm_bd(sv['u'], lam, "tn", "l0_s5_bu_dw")
    gr['s5'] = (db_in, dc_out, da_row, dd)
    do, dgate, gr['onorm_g_a'] = _onorm_bwd(sv['o'], sv['gate'], p['onorm_g'], dya, "l0_onorm_bwd")
    dq, dk, dv, dgb = riders.run("l0_gdr_bwd", _gdr_bwd, sv['q'], sv['k'], sv['v'], sv['gb'], sv['tm'], sv['s'], do)
    conv = p['conv_qkv']
    dhq_q, dcw_q = _qkv_pre_bwd(sv['hq'], conv, dq, 0, 4, True, HEAD_A ** -0.5, "l0_q_pre_bwd")
    dhq_k, dcw_k = _qkv_pre_bwd(sv['hq'], conv, dk, 4, 4, True, 1.0, "l0_k_pre_bwd")
    dhq_v, dcw_v = _qkv_pre_bwd(sv['hq'], conv, dv, 8, 4, False, 1.0, "l0_v_pre_bwd")
    gr['conv_qkv_a'] = jnp.concatenate([dcw_q, dcw_k, dcw_v], axis=1)
    dhq = jnp.concatenate([dhq_q, dhq_k, dhq_v], axis=1)
    dba, da_log, ddt_bias = _gates_bwd(sv['ba'], p['arow'], p['brow'], dgb, "l0_gates_bwd")
    gr['a_log_a'], gr['dt_bias_a'] = da_log[:, 4:8], ddt_bias[:, 4:8]
    dw_qkv_t = _mm(dhq, xn, "tn", "l0_in_qkv_dw", out_dtype=BF16)
    dw_gate_t = _mm(dgate, xn, "tn", "l0_in_gate_dw", out_dtype=BF16)
    dw_ba_t = _mm(dba, xn, "tn", "l0_in_ba_dw", out_dtype=BF16)
    dw_u_t = _mm(du, xn, "tn", "l0_in_u_dw", out_dtype=BF16)
    dw_in_t = _as_pieces(jnp.concatenate([dw_qkv_t, dw_gate_t, dw_ba_t[:8], dw_u_t], axis=0))
    riders.grad('w_in_t', jnp.concatenate(
        [dw_in_t, jnp.zeros((N_DEV, dict(PIECES)['w_in_t'] - W_IN_PIECE, D_MODEL), BF16)], axis=1))
    dxn = riders.run("l0_in_qkv_dx", _mm, dhq, wts['w_qkv_t'], "nn")
    dxn = _mm(dgate, wts['w_gate_t'], "nn", "l0_in_gate_dx", res=dxn)
    dxn = _mm(dba, wts['w_ba_t'], "nn", "l0_in_ba_dx", res=dxn)
    dxn = _mm(du, wts['w_u_t'], "nn", "l0_in_u_dx", res=dxn)
    return dxn, gr


def _xa_fwd(x1, g, mem_n, wq, wkv_t, wo, tag, riders):
    xq = _rms_fwd(x1, g, BF16, tag + "_norm")
    q = _mm(xq, wq, "nn", tag + "_q", out_dtype=BF16)
    kv = _mm(mem_n, wkv_t, "nt", tag + "_kv", out_dtype=BF16)
    o = riders.run(tag + "_attn", _attn_fwd, q, kv)
    x2 = _mm(o, wo, "nn", tag + "_o", res=x1)
    return x2, dict(xq=xq, q=q, kv=kv, o=o)


def _xa_bwd(dx2, x1, g, mem_n, wq, wkv_t, wo, sv, tag, layer, riders):
    do = _mm(dx2, wo, "nt", tag + "_o_dx", out_dtype=BF16)
    riders.grad('wo%d' % layer, _as_pieces(_mm(sv['o'], dx2, "tn", tag + "_o_dw", out_dtype=BF16)))
    dq, dk, dv = _attn_bwd(sv['q'], sv['kv'], do, tag + "_attn_bwd")
    dkv = jnp.concatenate([dk, dv], axis=1).astype(BF16)
    dxq = _mm(dq, wq, "nt", tag + "_q_dx")
    riders.grad('wq%d' % layer, _as_pieces(_mm(sv['xq'], dq, "tn", tag + "_q_dw", out_dtype=BF16)))
    dmem_n = _mm(dkv, wkv_t, "nn", tag + "_kv_dx")
    riders.grad('wkv_t%d' % layer, _as_pieces(_mm(dkv, mem_n, "tn", tag + "_kv_dw", out_dtype=BF16)))
    dx1, dg = riders.run(tag + "_norm_bwd", _rms_bwd, x1, g, dxq, dx2)
    return dx1, dmem_n, dg


def _ffn_fwd(x2, g, w_up_t, conv, w_down, tag, riders):
    xf = _rms_fwd(x2, g, BF16, tag + "_norm")
    h = riders.run(tag + "_up", _mm, xf, w_up_t, "nt")
    a = riders.run(tag + "_act", _ffn_act_fwd, h, conv)
    x3 = _mm(a, w_down, "nn", tag + "_down", res=x2)
    return x3, dict(xf=xf, h=h, a=a)


def _ffn_bwd(dx3, x2, g, w_up_t, conv, w_down, sv, tag, layer, riders):
    da = _mm(dx3, w_down, "nt", tag + "_down_dx")
    riders.grad('down%d' % layer, _as_pieces(_mm(sv['a'], dx3, "tn", tag + "_down_dw", out_dtype=BF16)))
    dh, dconv = riders.run(tag + "_act_bwd", _ffn_act_bwd, sv['h'], conv, da)
    dxf = riders.run(tag + "_up_dx", _mm, dh, w_up_t, "nn")
    dw_up_t = riders.run(tag + "_up_dw", _mm, dh, sv['xf'], "tn", out_dtype=BF16)
    riders.grad('up_t%d' % layer, _as_pieces(dw_up_t))
    dx2, dg = riders.run(tag + "_norm_bwd", _rms_bwd, x2, g, dxf, dx3)
    return dx2, dconv, dg


BIG_NAMES, SMALL_NAMES, REP_NAMES = list(BIG_SHARDED), list(SMALL_SHARDED), list(REPLICATED)
BIG_SIZES = [int(np.prod(_shard_shape(*BIG_SHARDED[n]))) for n in BIG_NAMES]
SMALL_SIZES = [int(np.prod(_shard_shape(*SMALL_SHARDED[n]))) for n in SMALL_NAMES]


PIECES = [('w_in_t', 384), ('w_glu', 32), ('w_out', 128), ('pool_w', 32), ('wq0', 128), ('wq1', 128),
          ('wkv_t0', 256), ('wkv_t1', 256), ('wo0', 128), ('wo1', 128), ('up_t0', 704), ('up_t1', 704),
          ('down0', 352), ('down1', 352)]
PIECE_OFFS = dict(zip([k for k, _ in PIECES], np.concatenate([[0], np.cumsum([r for _, r in PIECES])[:-1]]).tolist()))
W_IN_ROWS = 4 * WIDTH_A + 2 * N_HEADS_A + SSM_WIDTH
W_IN_PIECE = W_IN_ROWS // N_DEV


def _row_tile(rows):
    return max(t for t in range(16, min(rows, 512) + 1, 16) if rows % t == 0)


class _Riders:
    def __init__(self):
        self.waiting = {}
        self.grads = {}
        self.groups = []
        self.reduced = {}

    def add(self, host, comm, then):
        self.waiting.setdefault(host, []).append((comm, then))

    def run(self, name, fn, *args, **kw):
        riders = self.waiting.pop(name, [])
        if not riders:
            return fn(*args, name=name, **kw)
        out, couts = fn(*args, name=name, comm=[c for c, _ in riders], **kw)
        for (_, then), got in zip(riders, couts):
            then(got)
        return out

    def exchange(self, comm, host, name, then):
        if host is None:
            then(_comm_only(comm, name))
        else:
            self.add(host, comm, then)

    def grad(self, key, pieces):
        self.grads[key] = pieces
        for group in [g for g in self.groups if all(k in self.grads for k in g[1])]:
            self.groups.remove(group)
            self._reduce(*group)

    def _reduce(self, name, keys, swap_host, chips_host):
        arrays = [self.grads[k] for k in keys]
        rows = sum(a.shape[1] for a in arrays)
        tile = _row_tile(rows)

        def after_chips(chip_sums, got):
            total = _chip_sum(chip_sums, got[0], name + "_chip_sum", tr=tile)
            off = 0
            for k, a in zip(keys, arrays):
                self.reduced[k] = total[off:off + a.shape[1]]
                off += a.shape[1]

        def after_swap(got):
            core = lax.axis_index("c")
            keep = jnp.concatenate(
                [lax.dynamic_index_in_dim(a.reshape(4, 2, a.shape[1], PACK_COLS), core, 1, keepdims=False)
                 for a in arrays], axis=1)
            chip_sums = _pair_sum(keep, got[0], name + "_pair_sum", tr=tile)
            self.exchange(_chips_comm(chip_sums), chips_host, name + "_to_chips",
                          functools.partial(after_chips, chip_sums))

        self.exchange(_swap_comm(arrays), swap_host, name + "_to_sibling", after_swap)


class _Weights:
    def __init__(self, inp):
        bf = lambda a: a.astype(BF16)
        local = {'w_in_t': bf(inp['w_in_ab'][0]).T, 'w_glu': bf(inp['w_glu_b'][0]), 'w_out': bf(inp['w_out_ab'][0]),
                 'pool_w': bf(inp['pool_w'][0]),
                 'small': _pack([inp[n] for n in SMALL_NAMES])}
        for l in range(2):
            local['wq%d' % l] = bf(inp['xa_wq'][l])
            local['wkv_t%d' % l] = bf(inp['xa_wkv'][l]).T
            local['wo%d' % l] = bf(inp['xa_wo'][l])
            local['up_t%d' % l] = bf(inp['ffn_w_up'][l]).T
            local['down%d' % l] = bf(inp['ffn_w_down'][l])
        self.local, self.full = local, {}

    def plan(self, keys):
        return _gather_comm([self.local[k] for k in keys])

    def land(self, keys, gathered):
        for k, g in zip(keys, gathered):
            if k == 'small':
                off = 0
                for n, size in zip(SMALL_NAMES, SMALL_SIZES):
                    self.full[n] = _merge_shards(g.reshape(N_DEV, -1)[:, off:off + size], *SMALL_SHARDED[n])
                    off += size
            elif k == 'pool_w':
                self.full[k] = jnp.swapaxes(g, 0, 1).reshape(len(POOL_WINDOWS), POOL_GROUP, POOL_GROUP)
            else:
                self.full[k] = g.reshape(N_DEV * g.shape[1], g.shape[2])


GATHER_FIRST = ['w_in_t', 'small']
GATHER_RIDES = [('l0_gdr_fwd', ['w_glu', 'w_out', 'wq0', 'wkv_t0', 'wo0', 'up_t0']),
                ('l0_s5_bu', ['pool_w', 'wq1']), ('l0_s5_scan', ['down0']), ('l0_s5_cx', ['wo1']),
                ('l0_xa_attn', ['wkv_t1']), ('l0_ffn_up', ['up_t1']), ('l0_ffn_act', ['down1'])]
GRAD_RIDES = [('g_down1', ['down1'], 'l1_ffn_act_bwd', 'l1_ffn_up_dx'),
              ('g_up1', ['up_t1'], 'l1_ffn_norm_bwd', 'l0_ffn_act_bwd'),
              ('g_xa1', ['wq1', 'wkv_t1', 'wo1', 'pool_w'], 'l1_mix_norm_bwd', 'l0_ffn_up_dx'),
              ('g_down0', ['down0'], 'l0_ffn_act_bwd', 'l0_ffn_up_dw'),
              ('g_up0', ['up_t0'], 'l0_ffn_norm_bwd', 'l0_gdr_bwd'),
              ('g_xa0', ['wq0', 'wkv_t0', 'wo0'], 'l0_xa_norm_bwd', 'l0_gdr_bwd'),
              ('g_out', ['w_out', 'w_glu'], 'l0_s5_cx_dx', 'l0_s5_scan_bwd'),
              ('g_in', ['w_in_t'], 'l0_in_qkv_dx', 'adamw_ffn_w_down')]


def _local_step(inp):
    f32_of = lambda n: inp[n].astype(F32)
    weights = _Weights(inp)
    riders = _Riders()
    riders.groups = list(GRAD_RIDES)
    full = weights.full
    weights.land(GATHER_FIRST, _comm_only(weights.plan(GATHER_FIRST), "gather_first"))
    for host, keys in GATHER_RIDES:
        riders.add(host, weights.plan(keys), functools.partial(weights.land, keys))
    w_in_t = full['w_in_t']
    wts0 = dict(w_qkv_t=w_in_t[:3 * WIDTH_A], w_gate_t=w_in_t[3 * WIDTH_A:4 * WIDTH_A],
                w_ba_t=jnp.concatenate([w_in_t[4 * WIDTH_A:4 * WIDTH_A + 8], jnp.zeros((LANE - 8, D_MODEL), BF16)], 0),
                w_u_t=w_in_t[4 * WIDTH_A + 8:])
    lb_disc, disc_vjp = jax.vjp(_s5_discretise, f32_of('ssm_lambda_re')[0], f32_of('ssm_lambda_im')[0],
                                f32_of('ssm_b_re')[0], f32_of('ssm_b_im')[0], f32_of('ssm_log_dt')[0])
    b_in, c_out, a_row = _s5_matrices(*lb_disc, f32_of('ssm_c_re')[0], f32_of('ssm_c_im')[0])
    zeros4 = jnp.zeros((1, 4), F32)
    p0 = dict(conv_qkv=full['conv_qkv_a'][0], onorm_g=f32_of('onorm_g_a'),
              arow=jnp.concatenate([zeros4, f32_of('a_log_a'), jnp.zeros((1, LANE - 8), F32)], 1),
              brow=jnp.concatenate([zeros4, f32_of('dt_bias_a'), jnp.zeros((1, LANE - 8), F32)], 1),
              b_in=b_in.astype(BF16), c_out=c_out.astype(BF16), a_row=a_row,
              d_row=f32_of('ssm_d').reshape(1, SSM_WIDTH), b_glu=f32_of('b_glu_b'))

    x0 = inp['x'][0]
    mem_n = _rms_fwd(inp['mem'][0], inp['norm_mem_g'], BF16, "mem_norm")
    xn0 = _rms_fwd(x0, inp['norm_mix_g'][0], BF16, "l0_mix_norm")
    x1, sv_mix0 = _hybrid_fwd(xn0, x0, wts0, p0, weights, riders)
    x2, sv_xa0 = _xa_fwd(x1, inp['norm_xa_g'][0], mem_n, full['wq0'], full['wkv_t0'], full['wo0'], "l0_xa", riders)
    x3, sv_ffn0 = _ffn_fwd(x2, inp['norm_ffn_g'][0], full['up_t0'], full['ffn_conv'][0], full['down0'], "l0_ffn", riders)
    xn1 = _rms_fwd(x3, inp['norm_mix_g'][1], F32, "l1_mix_norm")
    x4 = _pool_fwd(xn1, full['pool_w'], full['pool_scale'], x3, "l1_pool")
    x5, sv_xa1 = _xa_fwd(x4, inp['norm_xa_g'][1], mem_n, full['wq1'], full['wkv_t1'], full['wo1'], "l1_xa", riders)
    x6, sv_ffn1 = _ffn_fwd(x5, inp['norm_ffn_g'][1], full['up_t1'], full['ffn_conv'][1], full['down1'], "l1_ffn", riders)
    loss_part, dx6, dg_final = _loss_head(x6, inp['norm_final_g'], inp['loss_target'][0], "loss_head")

    dx5, dconv1, dg_ffn1 = _ffn_bwd(dx6, x5, inp['norm_ffn_g'][1], full['up_t1'], full['ffn_conv'][1], full['down1'],
                                    sv_ffn1, "l1_ffn", 1, riders)
    dx4, dmem1, dg_xa1 = _xa_bwd(dx5, x4, inp['norm_xa_g'][1], mem_n, full['wq1'], full['wkv_t1'], full['wo1'],
                                 sv_xa1, "l1_xa", 1, riders)
    dxn1, dpool_w, dpool_scale = _pool_bwd(xn1, full['pool_w'], full['pool_scale'], dx4, "l1_pool_bwd")
    pool_pieces = jnp.swapaxes(dpool_w.astype(BF16).reshape(len(POOL_WINDOWS), N_DEV, -1, POOL_GROUP), 0, 1)
    riders.grad('pool_w', pool_pieces.reshape(N_DEV, -1, PACK_COLS))
    dx3, dg_mix1 = riders.run("l1_mix_norm_bwd", _rms_bwd, x3, inp['norm_mix_g'][1], dxn1, dx4)
    dx2, dconv0, dg_ffn0 = _ffn_bwd(dx3, x2, inp['norm_ffn_g'][0], full['up_t0'], full['ffn_conv'][0], full['down0'],
                                    sv_ffn0, "l0_ffn", 0, riders)
    dx1, dmem0, dg_xa0 = _xa_bwd(dx2, x1, inp['norm_xa_g'][0], mem_n, full['wq0'], full['wkv_t0'], full['wo0'],
                                 sv_xa0, "l0_xa", 0, riders)
    dxn0, g_mix0 = _hybrid_bwd(dx1, xn0, wts0, p0, sv_mix0, riders)
    grad_x, dg_mix0 = _rms_bwd(x0, inp['norm_mix_g'][0], dxn0, dx1, "l0_mix_norm_bwd")
    _, dg_mem = _rms_bwd(inp['mem'][0], inp['norm_mem_g'], dmem0 + dmem1, None, "mem_norm_bwd")
    assert not riders.groups and all(k.startswith("adamw_") for k in riders.waiting), (list(riders.waiting), riders.groups)

    db_in, dc_out, da_row, dd = g_mix0['s5']
    dlb_re, dlb_im, dbb_re, dbb_im, dc_re, dc_im = _s5_matrix_grads(db_in, dc_out, da_row)
    dlam_re, dlam_im, dbr, dbi, dlog_dt = disc_vjp((dlb_re, dlb_im, dbb_re, dbb_im))

    rep_grads = {
        'norm_mix_g': jnp.concatenate([dg_mix0, dg_mix1], 0), 'norm_xa_g': jnp.concatenate([dg_xa0, dg_xa1], 0),
        'norm_ffn_g': jnp.concatenate([dg_ffn0, dg_ffn1], 0), 'norm_mem_g': dg_mem.reshape(-1),
        'norm_final_g': dg_final.reshape(-1), 'a_log_a': g_mix0['a_log_a'], 'dt_bias_a': g_mix0['dt_bias_a'],
        'onorm_g_a': g_mix0['onorm_g_a'], 'ssm_lambda_re': dlam_re[None], 'ssm_lambda_im': dlam_im[None],
        'ssm_b_re': dbr[None], 'ssm_b_im': dbi[None], 'ssm_c_re': dc_re[None], 'ssm_c_im': dc_im[None],
        'ssm_d': dd.reshape(1, N_GROUPS, SSM_GROUP), 'ssm_log_dt': dlog_dt[None], 'b_glu_b': g_mix0['b_glu_b']}
    small_grads = {'conv_qkv_a': g_mix0['conv_qkv_a'][None], 'pool_scale': dpool_scale,
                   'ffn_conv': jnp.stack([dconv0, dconv1])}
    return loss_part, grad_x, riders, rep_grads, small_grads


ADAMW_ORDER = ['ffn_w_up', 'ffn_w_down', 'xa_wkv', 'xa_wq', 'xa_wo', 'w_out_ab', 'w_glu_b', 'pool_w', 'w_in_ab']
SMALL_GRADS_RIDE_ON = "adamw_ffn_w_up"


def _update(inp, loss_part, grad_x, riders, rep_grads, small_grads):
    dev = _device_index()
    gathered = {}
    misc_local = _pack([rep_grads[n] for n in REP_NAMES] + [small_grads[n] for n in SMALL_NAMES] + [loss_part])
    riders.add(SMALL_GRADS_RIDE_ON, _gather_comm([misc_local]), lambda got: gathered.update(misc=got[0]))
    piece = lambda key: riders.reduced[key]
    both = lambda name: jnp.stack([piece(name + '0'), piece(name + '1')])
    swap = lambda a: jnp.swapaxes(a, -1, -2)
    reduced = {'w_in_ab': lambda: piece('w_in_t')[:W_IN_PIECE][None],
               'w_glu_b': lambda: piece('w_glu').reshape(inp['w_glu_b'].shape),
               'w_out_ab': lambda: piece('w_out')[None], 'pool_w': lambda: piece('pool_w').reshape(inp['pool_w'].shape),
               'xa_wq': lambda: both('wq'), 'xa_wkv': lambda: both('wkv_t'), 'xa_wo': lambda: both('wo'),
               'ffn_w_up': lambda: both('up_t'), 'ffn_w_down': lambda: both('down')}
    transposed = ('w_in_ab', 'xa_wkv', 'ffn_w_up')
    grads, upd = {}, {}
    assert sorted(ADAMW_ORDER) == sorted(BIG_NAMES)
    for n in ADAMW_ORDER:
        fix = swap if n in transposed else (lambda a: a)
        g = reduced[n]()
        out = riders.run("adamw_" + n, _adamw, fix(inp[n]), g, fix(inp['m_' + n]), fix(inp['v_' + n]))
        upd[n], grads[n] = tuple(fix(o) for o in out), fix(g)
    assert not riders.waiting, list(riders.waiting)
    misc_sum = _sum_leading(gathered['misc'], "small_grads_sum")
    misc = _unpack(misc_sum, [inp[n].shape for n in REP_NAMES] + [SMALL_SHARDED[n][0] for n in SMALL_NAMES] + [()])
    loss = misc.pop()
    for n, g in zip(REP_NAMES, misc):
        grads[n] = g
    for n, g in zip(SMALL_NAMES, misc[len(REP_NAMES):]):
        grads[n] = lax.dynamic_index_in_dim(_split_shards(g, SMALL_SHARDED[n][1]), dev, 0, keepdims=False
                                            ).reshape(inp[n].shape)
    tiny_names = REP_NAMES + SMALL_NAMES
    rep_total = sum(int(np.prod(inp[n].shape)) for n in REP_NAMES)
    packs = [_pack([inp[prefix + n] for n in tiny_names]) for prefix in ('', 'm_', 'v_')]
    g_pack = _pack([misc_sum.reshape(-1)[:rep_total]] + [grads[n] for n in SMALL_NAMES])
    tiny_out = [_unpack(o, [inp[n].shape for n in tiny_names])
                for o in _adamw(packs[0], g_pack, packs[1], packs[2], "adamw_small")]
    for i, n in enumerate(tiny_names):
        upd[n] = tuple(o[i] for o in tiny_out)

    outs = [loss, grad_x[None]]
    outs += [grads[n] for n in WEIGHT_NAMES]
    for i in range(3):
        outs += [upd[n][i] for n in WEIGHT_NAMES]
    return tuple(outs)


def _step(inp):
    loss_part, grad_x, riders, rep_grads, small_grads = _local_step(inp)
    return _update(inp, loss_part, grad_x, riders, rep_grads, small_grads)


INPUT_NAMES = (['x', 'mem'] + WEIGHT_NAMES + ['loss_target'] + ['m_' + n for n in WEIGHT_NAMES]
               + ['v_' + n for n in WEIGHT_NAMES])


def kernel(x, mem, norm_mix_g, norm_xa_g, norm_ffn_g, norm_mem_g, norm_final_g, w_in_ab, conv_qkv_a, a_log_a, dt_bias_a, onorm_g_a, ssm_lambda_re, ssm_lambda_im, ssm_b_re, ssm_b_im, ssm_c_re, ssm_c_im, ssm_d, ssm_log_dt, w_glu_b, b_glu_b, w_out_ab, pool_w, pool_scale, xa_wq, xa_wkv, xa_wo, ffn_w_up, ffn_conv, ffn_w_down, loss_target, m_norm_mix_g, m_norm_xa_g, m_norm_ffn_g, m_norm_mem_g, m_norm_final_g, m_w_in_ab, m_conv_qkv_a, m_a_log_a, m_dt_bias_a, m_onorm_g_a, m_ssm_lambda_re, m_ssm_lambda_im, m_ssm_b_re, m_ssm_b_im, m_ssm_c_re, m_ssm_c_im, m_ssm_d, m_ssm_log_dt, m_w_glu_b, m_b_glu_b, m_w_out_ab, m_pool_w, m_pool_scale, m_xa_wq, m_xa_wkv, m_xa_wo, m_ffn_w_up, m_ffn_conv, m_ffn_w_down, v_norm_mix_g, v_norm_xa_g, v_norm_ffn_g, v_norm_mem_g, v_norm_final_g, v_w_in_ab, v_conv_qkv_a, v_a_log_a, v_dt_bias_a, v_onorm_g_a, v_ssm_lambda_re, v_ssm_lambda_im, v_ssm_b_re, v_ssm_b_im, v_ssm_c_re, v_ssm_c_im, v_ssm_d, v_ssm_log_dt, v_w_glu_b, v_b_glu_b, v_w_out_ab, v_pool_w, v_pool_scale, v_xa_wq, v_xa_wkv, v_xa_wo, v_ffn_w_up, v_ffn_conv, v_ffn_w_down):
    args = (x, mem, norm_mix_g, norm_xa_g, norm_ffn_g, norm_mem_g, norm_final_g, w_in_ab, conv_qkv_a, a_log_a, dt_bias_a, onorm_g_a, ssm_lambda_re, ssm_lambda_im, ssm_b_re, ssm_b_im, ssm_c_re, ssm_c_im, ssm_d, ssm_log_dt, w_glu_b, b_glu_b, w_out_ab, pool_w, pool_scale, xa_wq, xa_wkv, xa_wo, ffn_w_up, ffn_conv, ffn_w_down, loss_target, m_norm_mix_g, m_norm_xa_g, m_norm_ffn_g, m_norm_mem_g, m_norm_final_g, m_w_in_ab, m_conv_qkv_a, m_a_log_a, m_dt_bias_a, m_onorm_g_a, m_ssm_lambda_re, m_ssm_lambda_im, m_ssm_b_re, m_ssm_b_im, m_ssm_c_re, m_ssm_c_im, m_ssm_d, m_ssm_log_dt, m_w_glu_b, m_b_glu_b, m_w_out_ab, m_pool_w, m_pool_scale, m_xa_wq, m_xa_wkv, m_xa_wo, m_ffn_w_up, m_ffn_conv, m_ffn_w_down, v_norm_mix_g, v_norm_xa_g, v_norm_ffn_g, v_norm_mem_g, v_norm_final_g, v_w_in_ab, v_conv_qkv_a, v_a_log_a, v_dt_bias_a, v_onorm_g_a, v_ssm_lambda_re, v_ssm_lambda_im, v_ssm_b_re, v_ssm_b_im, v_ssm_c_re, v_ssm_c_im, v_ssm_d, v_ssm_log_dt, v_w_glu_b, v_b_glu_b, v_w_out_ab, v_pool_w, v_pool_scale, v_xa_wq, v_xa_wkv, v_xa_wo, v_ffn_w_up, v_ffn_conv, v_ffn_w_down)
    return _step(dict(zip(INPUT_NAMES, args)))
```

```python
import functools
import math

import numpy as np
import jax
import jax.numpy as jnp
from jax import lax
from jax.experimental import pallas as pl
from jax.experimental.pallas import tpu as pltpu

F32, BF16 = jnp.float32, jnp.bfloat16
HIGH, HIGHEST = lax.Precision.HIGH, lax.Precision.HIGHEST
MESH = pl.DeviceIdType.MESH

N_DEV = 8
SEQ, D_MODEL, MEM_LEN = 2048, 1024, 256
WIDTH_A, N_HEADS_A, HEAD_A, CONV_A = 512, 4, 128, 4
GDR_CHUNK = 128
GDR_HEADS_PER_STEP = 4
SSM_WIDTH, SSM_GROUP, N_GROUPS, SSM_STATE = 512, 16, 32, 64
SSM_CH = N_GROUPS * SSM_STATE
SCAN_CB = 512
POOL_WINDOWS = (2, 4, 8, 16)
POOL_GROUP = 256
N_HEADS_X, HEAD_X = 4, 256
D_FF, CONV_FFN = 2816, 3
RMS_EPS = 1e-6
ADAM_LR, ADAM_B1, ADAM_B2, ADAM_EPS, ADAM_WD, ADAM_STEP = 0.001, 0.9, 0.999, 1e-08, 0.01, 10
LANE = 128
PACK_COLS = 1024
VMEM_LIMIT_BYTES = 56 * 1024 * 1024


def _params(sem=None):
    return pltpu.CompilerParams(dimension_semantics=sem, vmem_limit_bytes=VMEM_LIMIT_BYTES)


class Comm:
    def __init__(self, inputs, out_shapes, sems, start, end, mid=None):
        self.inputs, self.out_shapes, self.sems = list(inputs), list(out_shapes), list(sems)
        self.start, self.mid, self.end = start, mid, end


def _merge_comms(comms):
    comms = [c for c in comms if c is not None]
    if not comms:
        return None, []
    bounds, ni, no, ns = [], 0, 0, 0
    for c in comms:
        bounds.append((ni, no, ns))
        ni, no, ns = ni + len(c.inputs), no + len(c.out_shapes), ns + len(c.sems)

    def phase(which):
        def run(ins, outs, sems):
            for c, (i0, o0, s0) in zip(comms, bounds):
                fn = getattr(c, which)
                if fn is not None:
                    fn(ins[i0:i0 + len(c.inputs)], outs[o0:o0 + len(c.out_shapes)], sems[s0:s0 + len(c.sems)])
        return run

    merged = Comm([a for c in comms for a in c.inputs], [s for c in comms for s in c.out_shapes],
                  [s for c in comms for s in c.sems], phase("start"), phase("end"), phase("mid"))
    return merged, [(o0, o0 + len(c.out_shapes)) for c, (_, o0, _) in zip(comms, bounds)]


def _call(body, *, name, grid, in_specs, out_specs, out_shape, args, scratch_shapes=(), sem=None, comm=None):
    single = not isinstance(out_shape, (list, tuple))
    out_specs_l = [out_specs] if single else list(out_specs)
    out_shape_l = [out_shape] if single else list(out_shape)
    scratch_shapes = list(scratch_shapes)
    merged, spans = _merge_comms(comm if isinstance(comm, (list, tuple)) else [comm])
    if merged is None:
        outs = pl.pallas_call(body, name=name, grid=grid, in_specs=list(in_specs), out_specs=out_specs_l,
                              out_shape=out_shape_l, scratch_shapes=scratch_shapes, compiler_params=_params(sem))(*args)
        outs = outs[0] if single else outs
        return outs if comm is None else (outs, [])
    n_in, n_out, n_scr = len(in_specs), len(out_specs_l), len(scratch_shapes)
    ci, co = len(merged.inputs), len(merged.out_shapes)
    total = int(np.prod(grid))

    def wrapped(*refs):
        ins, cins = refs[:n_in], refs[n_in:n_in + ci]
        outs, couts = refs[n_in + ci:n_in + ci + n_out], refs[n_in + ci + n_out:n_in + ci + n_out + co]
        scr, csems = refs[n_in + ci + n_out + co:n_in + ci + n_out + co + n_scr], refs[n_in + ci + n_out + co + n_scr:]
        lin = pl.program_id(0)
        for d in range(1, len(grid)):
            lin = lin * grid[d] + pl.program_id(d)
        pl.when(lin == 0)(lambda: merged.start(cins, couts, csems))
        body(*ins, *outs, *scr)
        mid_step = min((3 * total) // 4, total - 1)
        pl.when(lin == mid_step)(lambda: merged.mid(cins, couts, csems))
        pl.when(lin == total - 1)(lambda: merged.end(cins, couts, csems))

    any_spec = pl.BlockSpec(memory_space=pl.ANY)
    res = pl.pallas_call(
        wrapped, name=name, grid=grid, in_specs=list(in_specs) + [any_spec] * ci,
        out_specs=out_specs_l + [any_spec] * co, out_shape=out_shape_l + merged.out_shapes,
        scratch_shapes=scratch_shapes + merged.sems,
        compiler_params=_params(("arbitrary",) * len(grid)))(*args, *merged.inputs)
    outs, couts = res[:n_out], res[n_out:]
    return (outs[0] if single else list(outs)), [list(couts[a:b]) for a, b in spans]


def _comm_only(comm, name):
    def body():
        pass

    _, couts = _call(body, name=name, grid=(1,), in_specs=[], out_specs=[], out_shape=[], args=[], comm=comm)
    return couts[0]


def _tile(dim, pref):
    best = None
    for t in range(LANE, min(dim, pref) + 1, LANE):
        if dim % t == 0:
            best = t
    return best if best is not None else dim


MM_VMEM_BUDGET = 40 * 1024 * 1024


def _mm_tiles(m, n, k, a_bytes, b_bytes, o_bytes, r_bytes):
    for tk in (k, _tile(k, 2048), _tile(k, 1024), _tile(k, 512)):
        for tm, tn in ((1024, 1536), (1024, 1024), (1024, 512), (512, 512), (256, 512), (256, 256)):
            tm, tn = _tile(m, tm), _tile(n, tn)
            acc = 0 if tk == k else tm * tn * 4
            need = 2 * (tm * tk * a_bytes + tk * tn * b_bytes + tm * tn * (o_bytes + r_bytes)) + acc
            if need <= MM_VMEM_BUDGET:
                return tm, tn, tk
    raise ValueError("no matmul tiling fits VMEM")


def _mm(a, b, mode, name, out_dtype=F32, res=None, comm=None):
    if mode == "nn":
        (m, k), n = a.shape, b.shape[1]
    elif mode == "nt":
        (m, k), n = a.shape, b.shape[0]
    else:
        (k, m), n = a.shape, b.shape[1]
    tm, tn, tk = _mm_tiles(m, n, k, a.dtype.itemsize, b.dtype.itemsize, jnp.dtype(out_dtype).itemsize,
                           0 if res is None else res.dtype.itemsize)
    nk = k // tk
    dims = {"nn": ((1,), (0,)), "nt": ((1,), (1,)), "tn": ((0,), (0,))}[mode]

    def body(*refs):
        if res is None:
            a_ref, b_ref, o_ref = refs[:3]
            r_ref = None
        else:
            a_ref, b_ref, r_ref, o_ref = refs[:4]
        part = lax.dot_general(a_ref[...].astype(BF16), b_ref[...].astype(BF16), (dims, ((), ())),
                               preferred_element_type=F32)

        def finish(out):
            if r_ref is not None:
                out = out + r_ref[...].astype(F32)
            o_ref[...] = out.astype(out_dtype)

        if nk == 1:
            finish(part)
            return
        acc = refs[-1]
        kk = pl.program_id(2)

        @pl.when(kk == 0)
        def _():
            acc[...] = part

        @pl.when(kk > 0)
        def _():
            acc[...] += part

        @pl.when(kk == nk - 1)
        def _():
            finish(acc[...])

    a_spec = (pl.BlockSpec((tk, tm), lambda i, j, q: (q, i)) if mode == "tn"
              else pl.BlockSpec((tm, tk), lambda i, j, q: (i, q)))
    b_spec = (pl.BlockSpec((tn, tk), lambda i, j, q: (j, q)) if mode == "nt"
              else pl.BlockSpec((tk, tn), lambda i, j, q: (q, j)))
    o_spec = pl.BlockSpec((tm, tn), lambda i, j, q: (i, j))
    in_specs, args = [a_spec, b_spec], [a, b]
    if res is not None:
        in_specs.append(o_spec)
        args.append(res)
    return _call(body, name=name, grid=(m // tm, n // tn, nk), in_specs=in_specs, out_specs=o_spec,
                 out_shape=jax.ShapeDtypeStruct((m, n), out_dtype),
                 scratch_shapes=[] if nk == 1 else [pltpu.VMEM((tm, tn), F32)],
                 sem=("parallel", "parallel", "arbitrary"), args=args, comm=comm)


def _mm_bd(a, b, mode, name, out_dtype=F32, res=None, comm=None, tm=1024):
    if mode == "tn":
        k = a.shape[0]
        nb = min(a.shape[1], b.shape[1]) // LANE
        ma, n = a.shape[1] // nb, b.shape[1] // nb

        def body(a_ref, b_ref, o_ref):
            o_ref[0] = lax.dot_general(a_ref[...].astype(BF16), b_ref[...].astype(BF16), (((0,), (0,)), ((), ())),
                                       preferred_element_type=F32).astype(out_dtype)

        return _call(body, name=name, grid=(nb,),
                     in_specs=[pl.BlockSpec((k, ma), lambda j: (0, j)), pl.BlockSpec((k, n), lambda j: (0, j))],
                     out_specs=pl.BlockSpec((1, ma, n), lambda j: (j, 0, 0)),
                     out_shape=jax.ShapeDtypeStruct((nb, ma, n), out_dtype), sem=("parallel",), args=(a, b), comm=comm)
    m = a.shape[0]
    nb = b.shape[0]
    ka = a.shape[1] // nb
    n = b.shape[2] if mode == "nn" else b.shape[1]
    tm = _tile(m, tm)
    dims = ((1,), (0,)) if mode == "nn" else ((1,), (1,))

    def body(*refs):
        if res is None:
            a_ref, b_ref, o_ref = refs
            r_ref = None
        else:
            a_ref, b_ref, r_ref, o_ref = refs
        out = lax.dot_general(a_ref[...].astype(BF16), b_ref[0].astype(BF16), (dims, ((), ())),
                              preferred_element_type=F32)
        if r_ref is not None:
            out = out + r_ref[...].astype(F32)
        o_ref[...] = out.astype(out_dtype)

    o_spec = pl.BlockSpec((tm, n), lambda i, j: (i, j))
    in_specs = [pl.BlockSpec((tm, ka), lambda i, j: (i, j)), pl.BlockSpec((1,) + b.shape[1:], lambda i, j: (j, 0, 0))]
    args = [a, b]
    if res is not None:
        in_specs.append(o_spec)
        args.append(res)
    return _call(body, name=name, grid=(m // tm, nb), in_specs=in_specs, out_specs=o_spec,
                 out_shape=jax.ShapeDtypeStruct((m, nb * n), out_dtype), sem=("parallel", "parallel"),
                 args=args, comm=comm)


def _rms_fwd(x, g, out_dtype, name, tr=256):
    rows, d = x.shape

    def body(x_ref, g_ref, o_ref):
        xv = x_ref[...]
        r = lax.rsqrt(jnp.mean(xv * xv, axis=-1, keepdims=True) + RMS_EPS)
        o_ref[...] = (xv * r * g_ref[...]).astype(out_dtype)

    return pl.pallas_call(
        body, name=name, grid=(rows // tr,),
        in_specs=[pl.BlockSpec((tr, d), lambda i: (i, 0)), pl.BlockSpec((1, d), lambda i: (0, 0))],
        out_specs=pl.BlockSpec((tr, d), lambda i: (i, 0)), out_shape=jax.ShapeDtypeStruct((rows, d), out_dtype),
        compiler_params=_params(("parallel",)))(x, g.reshape(1, d))


def _rms_bwd(x, g, dy, dres, name, tr=256, comm=None):
    rows, d = x.shape

    def body(*refs):
        if dres is None:
            x_ref, g_ref, dy_ref, dx_ref, dg_ref = refs
            r_ref = None
        else:
            x_ref, g_ref, dy_ref, r_ref, dx_ref, dg_ref = refs

        @pl.when(pl.program_id(0) == 0)
        def _():
            dg_ref[...] = jnp.zeros_like(dg_ref)

        xv, dyv = x_ref[...], dy_ref[...].astype(F32)
        r = lax.rsqrt(jnp.mean(xv * xv, axis=-1, keepdims=True) + RMS_EPS)
        xh = xv * r
        dyg = dyv * g_ref[...]
        dx = r * (dyg - xh * jnp.mean(dyg * xh, axis=-1, keepdims=True))
        if r_ref is not None:
            dx = dx + r_ref[...]
        dx_ref[...] = dx
        dg_ref[...] += jnp.sum(dyv * xh, axis=0, keepdims=True)

    blk = pl.BlockSpec((tr, d), lambda i: (i, 0))
    vec = pl.BlockSpec((1, d), lambda i: (0, 0))
    in_specs, args = [blk, vec, blk], [x, g.reshape(1, d), dy]
    if dres is not None:
        in_specs.append(blk)
        args.append(dres)
    return _call(
        body, name=name, grid=(rows // tr,), in_specs=in_specs, out_specs=[blk, vec],
        out_shape=[jax.ShapeDtypeStruct((rows, d), F32), jax.ShapeDtypeStruct((1, d), F32)],
        sem=("arbitrary",), args=args, comm=comm)


def _loss_head(x, g, target, name, tr=256):
    rows, d = x.shape

    def body(x_ref, g_ref, t_ref, loss_ref, dx_ref, dg_ref):
        @pl.when(pl.program_id(0) == 0)
        def _():
            dg_ref[...] = jnp.zeros_like(dg_ref)
            loss_ref[...] = jnp.zeros_like(loss_ref)

        xv = x_ref[...]
        r = lax.rsqrt(jnp.mean(xv * xv, axis=-1, keepdims=True) + RMS_EPS)
        xh = xv * r
        err = xh * g_ref[...] - t_ref[...]
        loss_ref[...] += 0.5 * jnp.sum(jnp.mean(err * err, axis=-1, keepdims=True), keepdims=True)
        dyv = err * (1.0 / d)
        dyg = dyv * g_ref[...]
        dx_ref[...] = r * (dyg - xh * jnp.mean(dyg * xh, axis=-1, keepdims=True))
        dg_ref[...] += jnp.sum(dyv * xh, axis=0, keepdims=True)

    blk = pl.BlockSpec((tr, d), lambda i: (i, 0))
    vec = pl.BlockSpec((1, d), lambda i: (0, 0))
    return pl.pallas_call(
        body, name=name, grid=(rows // tr,), in_specs=[blk, vec, blk],
        out_specs=[pl.BlockSpec((1, 1), lambda i: (0, 0)), blk, vec],
        out_shape=[jax.ShapeDtypeStruct((1, 1), F32), jax.ShapeDtypeStruct((rows, d), F32),
                   jax.ShapeDtypeStruct((1, d), F32)],
        compiler_params=_params(("arbitrary",)))(x, g.reshape(1, d), target)


def _shift_down(x, s):
    rows = lax.broadcasted_iota(jnp.int32, x.shape, 0)
    return jnp.where(rows >= s, pltpu.roll(x, s, 0), 0.0)


def _shift_up(x, s):
    n = x.shape[0]
    rows = lax.broadcasted_iota(jnp.int32, x.shape, 0)
    return jnp.where(rows < n - s, pltpu.roll(x, n - s, 0), 0.0)


def _sigmoid(x):
    return 1.0 / (1.0 + jnp.exp(-x))


def _silu_and_grad(x):
    s = _sigmoid(x)
    return x * s, s * (1.0 + x * (1.0 - s))


_GELU_C0, _GELU_C1 = math.sqrt(2.0 / math.pi), 0.044715


def _gelu_and_grad(x):
    th = jnp.tanh(_GELU_C0 * (x + _GELU_C1 * x * x * x))
    y = 0.5 * x * (1.0 + th)
    dy = 0.5 * (1.0 + th) + 0.5 * x * (1.0 - th * th) * _GELU_C0 * (1.0 + 3.0 * _GELU_C1 * x * x)
    return y, dy


def _ffn_act_fwd(h, w, name, tc=256, comm=None):
    t = h.shape[0]
    nb = D_FF // tc

    def body(hg_ref, hv_ref, wg_ref, wv_ref, a_ref):
        def conv(x, wr):
            return wr[2:3, :] * x + wr[1:2, :] * _shift_down(x, 1) + wr[0:1, :] * _shift_down(x, 2)

        cg = conv(hg_ref[...], wg_ref[...])
        cv = conv(hv_ref[...], wv_ref[...])
        a_ref[...] = (cg * _sigmoid(cg) * cv).astype(BF16)

    return _call(
        body, name=name, grid=(nb,),
        in_specs=[pl.BlockSpec((t, tc), lambda j: (0, j)), pl.BlockSpec((t, tc), lambda j: (0, j + nb)),
                  pl.BlockSpec((CONV_FFN, tc), lambda j: (0, j)), pl.BlockSpec((CONV_FFN, tc), lambda j: (0, j + nb))],
        out_specs=pl.BlockSpec((t, tc), lambda j: (0, j)), out_shape=jax.ShapeDtypeStruct((t, D_FF), BF16),
        sem=("parallel",), args=(h, h, w, w), comm=comm)


def _ffn_act_bwd(h, w, da, name, tc=256, comm=None):
    t = h.shape[0]
    nb = D_FF // tc

    def body(hg_ref, hv_ref, wg_ref, wv_ref, da_ref, dhg_ref, dhv_ref, dwg_ref, dwv_ref):
        hg, hv, wg, wv = hg_ref[...], hv_ref[...], wg_ref[...], wv_ref[...]
        hg1, hg2, hv1, hv2 = _shift_down(hg, 1), _shift_down(hg, 2), _shift_down(hv, 1), _shift_down(hv, 2)
        cg = wg[2:3, :] * hg + wg[1:2, :] * hg1 + wg[0:1, :] * hg2
        cv = wv[2:3, :] * hv + wv[1:2, :] * hv1 + wv[0:1, :] * hv2
        sg, dsg = _silu_and_grad(cg)
        dav = da_ref[...].astype(F32)
        dcv = dav * sg
        dcg = dav * cv * dsg

        def conv_t(dc, wr):
            return wr[2:3, :] * dc + wr[1:2, :] * _shift_up(dc, 1) + wr[0:1, :] * _shift_up(dc, 2)

        dhg_ref[...] = conv_t(dcg, wg).astype(BF16)
        dhv_ref[...] = conv_t(dcv, wv).astype(BF16)
        dwg_ref[0:1, :] = jnp.sum(dcg * hg2, axis=0, keepdims=True)
        dwg_ref[1:2, :] = jnp.sum(dcg * hg1, axis=0, keepdims=True)
        dwg_ref[2:3, :] = jnp.sum(dcg * hg, axis=0, keepdims=True)
        dwv_ref[0:1, :] = jnp.sum(dcv * hv2, axis=0, keepdims=True)
        dwv_ref[1:2, :] = jnp.sum(dcv * hv1, axis=0, keepdims=True)
        dwv_ref[2:3, :] = jnp.sum(dcv * hv, axis=0, keepdims=True)

    big = lambda off: pl.BlockSpec((t, tc), lambda j: (0, j + off))
    small = lambda off: pl.BlockSpec((CONV_FFN, tc), lambda j: (0, j + off))
    res = _call(
        body, name=name, grid=(nb,),
        in_specs=[big(0), big(nb), small(0), small(nb), big(0)],
        out_specs=[big(0), big(0), small(0), small(0)],
        out_shape=[jax.ShapeDtypeStruct((t, D_FF), BF16), jax.ShapeDtypeStruct((t, D_FF), BF16),
                   jax.ShapeDtypeStruct((CONV_FFN, D_FF), F32), jax.ShapeDtypeStruct((CONV_FFN, D_FF), F32)],
        sem=("parallel",), args=(h, h, w, w, da), comm=comm)
    (dhg, dhv, dwg, dwv), couts = res if comm is not None else (res, None)
    out = (jnp.concatenate([dhg, dhv], axis=1), jnp.concatenate([dwg, dwv], axis=1))
    return out if comm is None else (out, couts)


def _attn_probs(q, k):
    s = lax.dot_general(q.astype(BF16), k.astype(BF16), (((1,), (1,)), ((), ())),
                        preferred_element_type=F32) * (HEAD_X ** -0.5)
    s = s - jnp.max(s, axis=-1, keepdims=True)
    p = jnp.exp(s)
    return p / jnp.sum(p, axis=-1, keepdims=True)


def _attn_fwd(q, kv, name, tq=512, comm=None):
    t = q.shape[0]

    def body(q_ref, k_ref, v_ref, o_ref):
        p = _attn_probs(q_ref[...], k_ref[...])
        o_ref[...] = jnp.dot(p.astype(BF16), v_ref[...].astype(BF16), preferred_element_type=F32).astype(BF16)

    return _call(
        body, name=name, grid=(N_HEADS_X, t // tq),
        in_specs=[pl.BlockSpec((tq, HEAD_X), lambda h, i: (i, h)),
                  pl.BlockSpec((MEM_LEN, HEAD_X), lambda h, i: (0, h)),
                  pl.BlockSpec((MEM_LEN, HEAD_X), lambda h, i: (0, h + N_HEADS_X))],
        out_specs=pl.BlockSpec((tq, HEAD_X), lambda h, i: (i, h)),
        out_shape=jax.ShapeDtypeStruct((t, N_HEADS_X * HEAD_X), BF16),
        sem=("parallel", "parallel"), args=(q, kv, kv), comm=comm)


def _attn_bwd(q, kv, do, name, tq=512):
    t = q.shape[0]

    def body(q_ref, k_ref, v_ref, do_ref, dq_ref, dk_ref, dv_ref):
        @pl.when(pl.program_id(1) == 0)
        def _():
            dk_ref[...] = jnp.zeros_like(dk_ref)
            dv_ref[...] = jnp.zeros_like(dv_ref)

        qb, kb, vb, dob = (r[...].astype(BF16) for r in (q_ref, k_ref, v_ref, do_ref))
        p = _attn_probs(qb, kb)
        dp = lax.dot_general(dob, vb, (((1,), (1,)), ((), ())), preferred_element_type=F32)
        ds = p * (dp - jnp.sum(dp * p, axis=-1, keepdims=True)) * (HEAD_X ** -0.5)
        dsb = ds.astype(BF16)
        dq_ref[...] = jnp.dot(dsb, kb, preferred_element_type=F32).astype(BF16)
        dk_ref[...] += lax.dot_general(dsb, qb, (((0,), (0,)), ((), ())), preferred_element_type=F32)
        dv_ref[...] += lax.dot_general(p.astype(BF16), dob, (((0,), (0,)), ((), ())), preferred_element_type=F32)

    qs = pl.BlockSpec((tq, HEAD_X), lambda h, i: (i, h))
    ms = pl.BlockSpec((MEM_LEN, HEAD_X), lambda h, i: (0, h))
    return pl.pallas_call(
        body, name=name, grid=(N_HEADS_X, t // tq),
        in_specs=[qs, ms, pl.BlockSpec((MEM_LEN, HEAD_X), lambda h, i: (0, h + N_HEADS_X)), qs],
        out_specs=[qs, ms, ms],
        out_shape=[jax.ShapeDtypeStruct((t, D_MODEL), BF16), jax.ShapeDtypeStruct((MEM_LEN, D_MODEL), F32),
                   jax.ShapeDtypeStruct((MEM_LEN, D_MODEL), F32)],
        compiler_params=_params(("parallel", "arbitrary")))(q, kv, kv, do)


def _pool_counts(t, win):
    pos = lax.broadcasted_iota(jnp.int32, (t, 1), 0).astype(F32) + 1.0
    return 1.0 / jnp.minimum(pos, float(win))


def _pool_delta(xv, win):
    s, step = xv, 1
    while step < win:
        s = s + _shift_down(s, step)
        step *= 2
    return s * _pool_counts(xv.shape[0], win) - xv


def _pool_delta_t(dv, win):
    s, step = dv * _pool_counts(dv.shape[0], win), 1
    while step < win:
        s = s + _shift_up(s, step)
        step *= 2
    return s - dv


def _pool_fwd(xn, w, scale, res, name):
    t = xn.shape[0]

    def make_branch(win, xn_ref, w_ref, s_ref, r_ref, o_ref):
        def branch():
            dl = _pool_delta(xn_ref[...], win)
            y = jnp.dot(dl.astype(BF16), w_ref[0], preferred_element_type=F32)
            o_ref[...] = r_ref[...] + y * s_ref[...]
        return branch

    def body(xn_ref, w_ref, s_ref, r_ref, o_ref):
        for gi, win in enumerate(POOL_WINDOWS):
            pl.when(pl.program_id(0) == gi)(make_branch(win, xn_ref, w_ref, s_ref, r_ref, o_ref))

    blk = pl.BlockSpec((t, POOL_GROUP), lambda g: (0, g))
    return pl.pallas_call(
        body, name=name, grid=(len(POOL_WINDOWS),),
        in_specs=[blk, pl.BlockSpec((1, POOL_GROUP, POOL_GROUP), lambda g: (g, 0, 0)),
                  pl.BlockSpec((1, POOL_GROUP), lambda g: (0, g)), blk],
        out_specs=blk, out_shape=jax.ShapeDtypeStruct((t, D_MODEL), F32),
        compiler_params=_params(("parallel",)))(xn, w, scale, res)


def _pool_bwd(xn, w, scale, dmix, name):
    t = xn.shape[0]

    def make_branch(win, xn_ref, w_ref, s_ref, d_ref, dxn_ref, dw_ref, ds_ref):
        def branch():
            dl = _pool_delta(xn_ref[...], win).astype(BF16)
            wv = w_ref[0]
            dm = d_ref[...]
            y = jnp.dot(dl, wv, preferred_element_type=F32)
            ds_ref[...] = jnp.sum(dm * y, axis=0, keepdims=True)
            dy = (dm * s_ref[...]).astype(BF16)
            dw_ref[0] = lax.dot_general(dl, dy, (((0,), (0,)), ((), ())), preferred_element_type=F32)
            ddl = lax.dot_general(dy, wv, (((1,), (1,)), ((), ())), preferred_element_type=F32)
            dxn_ref[...] = _pool_delta_t(ddl, win)
        return branch

    def body(*refs):
        for gi, win in enumerate(POOL_WINDOWS):
            pl.when(pl.program_id(0) == gi)(make_branch(win, *refs))

    blk = pl.BlockSpec((t, POOL_GROUP), lambda g: (0, g))
    wspec = pl.BlockSpec((1, POOL_GROUP, POOL_GROUP), lambda g: (g, 0, 0))
    vec = pl.BlockSpec((1, POOL_GROUP), lambda g: (0, g))
    return pl.pallas_call(
        body, name=name, grid=(len(POOL_WINDOWS),), in_specs=[blk, wspec, vec, blk], out_specs=[blk, wspec, vec],
        out_shape=[jax.ShapeDtypeStruct((t, D_MODEL), F32),
                   jax.ShapeDtypeStruct((len(POOL_WINDOWS), POOL_GROUP, POOL_GROUP), F32),
                   jax.ShapeDtypeStruct((1, D_MODEL), F32)],
        compiler_params=_params(("parallel",)))(xn, w, scale, dmix)


def _qkv_conv(h, wr):
    return (wr[3:4, :] * h + wr[2:3, :] * _shift_down(h, 1) + wr[1:2, :] * _shift_down(h, 2)
            + wr[0:1, :] * _shift_down(h, 3))


def _qkv_pre_fwd(h, w, col0, ncols, normalize, scale, name):
    t = h.shape[0]

    def body(h_ref, w_ref, o_ref):
        c = _qkv_conv(h_ref[...], w_ref[...])
        s = c * _sigmoid(c)
        if normalize:
            s = s * lax.rsqrt(jnp.sum(s * s, axis=-1, keepdims=True) + 1e-6) * scale
        o_ref[...] = s

    return pl.pallas_call(
        body, name=name, grid=(ncols,),
        in_specs=[pl.BlockSpec((t, HEAD_A), lambda j: (0, j + col0)), pl.BlockSpec((CONV_A, HEAD_A), lambda j: (0, j + col0))],
        out_specs=pl.BlockSpec((t, HEAD_A), lambda j: (0, j)), out_shape=jax.ShapeDtypeStruct((t, ncols * HEAD_A), F32),
        compiler_params=_params(("parallel",)))(h, w)


def _qkv_pre_bwd(h, w, dy, col0, ncols, normalize, scale, name):
    t = h.shape[0]

    def body(h_ref, w_ref, dy_ref, dh_ref, dw_ref):
        hv, wr, dyv = h_ref[...], w_ref[...], dy_ref[...]
        h1, h2, h3 = _shift_down(hv, 1), _shift_down(hv, 2), _shift_down(hv, 3)
        c = wr[3:4, :] * hv + wr[2:3, :] * h1 + wr[1:2, :] * h2 + wr[0:1, :] * h3
        s, dsilu = _silu_and_grad(c)
        if normalize:
            r = lax.rsqrt(jnp.sum(s * s, axis=-1, keepdims=True) + 1e-6)
            y = s * r
            dyv = dyv * scale
            ds = r * (dyv - y * jnp.sum(dyv * y, axis=-1, keepdims=True))
        else:
            ds = dyv
        dc = ds * dsilu
        dh = (wr[3:4, :] * dc + wr[2:3, :] * _shift_up(dc, 1) + wr[1:2, :] * _shift_up(dc, 2)
              + wr[0:1, :] * _shift_up(dc, 3))
        dh_ref[...] = dh.astype(BF16)
        dw_ref[0:1, :] = jnp.sum(dc * h3, axis=0, keepdims=True)
        dw_ref[1:2, :] = jnp.sum(dc * h2, axis=0, keepdims=True)
        dw_ref[2:3, :] = jnp.sum(dc * h1, axis=0, keepdims=True)
        dw_ref[3:4, :] = jnp.sum(dc * hv, axis=0, keepdims=True)

    return pl.pallas_call(
        body, name=name, grid=(ncols,),
        in_specs=[pl.BlockSpec((t, HEAD_A), lambda j: (0, j + col0)), pl.BlockSpec((CONV_A, HEAD_A), lambda j: (0, j + col0)),
                  pl.BlockSpec((t, HEAD_A), lambda j: (0, j))],
        out_specs=[pl.BlockSpec((t, HEAD_A), lambda j: (0, j)), pl.BlockSpec((CONV_A, HEAD_A), lambda j: (0, j))],
        out_shape=[jax.ShapeDtypeStruct((t, ncols * HEAD_A), BF16), jax.ShapeDtypeStruct((CONV_A, ncols * HEAD_A), F32)],
        compiler_params=_params(("parallel",)))(h, w, dy)


def _softplus(x):
    return jnp.maximum(x, 0.0) + jnp.log1p(jnp.exp(-jnp.abs(x)))


def _gates_fwd(ba, arow, brow, name):
    t = ba.shape[0]

    def body(x_ref, a_ref, b_ref, o_ref):
        xv = x_ref[...]
        lane = lax.broadcasted_iota(jnp.int32, xv.shape, 1)
        beta = _sigmoid(xv)
        g = -jnp.exp(a_ref[...]) * _softplus(xv + b_ref[...])
        o_ref[...] = jnp.where(lane < N_HEADS_A, beta, jnp.where(lane < 2 * N_HEADS_A, g, 0.0))

    return pl.pallas_call(body, name=name, out_shape=jax.ShapeDtypeStruct((t, LANE), F32),
                          compiler_params=_params())(ba, arow, brow)


def _gates_bwd(ba, arow, brow, dgb, name):
    t = ba.shape[0]

    def body(x_ref, a_ref, b_ref, d_ref, dx_ref, da_ref, db_ref):
        xv = x_ref[...]
        dv = d_ref[0] + d_ref[1] + d_ref[2] + d_ref[3]
        lane = lax.broadcasted_iota(jnp.int32, xv.shape, 1)
        beta = _sigmoid(xv)
        ea = jnp.exp(a_ref[...])
        z = xv + b_ref[...]
        dgv = jnp.where((lane >= N_HEADS_A) & (lane < 2 * N_HEADS_A), dv, 0.0) * (-ea)
        dz = dgv * _sigmoid(z)
        dx = jnp.where(lane < N_HEADS_A, dv * beta * (1.0 - beta), dz)
        dx_ref[...] = dx.astype(BF16)
        db_ref[...] = jnp.sum(dz, axis=0, keepdims=True)
        da_ref[...] = jnp.sum(dgv * _softplus(z), axis=0, keepdims=True)

    return pl.pallas_call(
        body, name=name,
        out_shape=[jax.ShapeDtypeStruct((t, LANE), BF16), jax.ShapeDtypeStruct((1, LANE), F32),
                   jax.ShapeDtypeStruct((1, LANE), F32)],
        compiler_params=_params())(ba, arow, brow, dgb)


def _dot(a, b, prec=None):
    if prec is None:
        return jnp.dot(a.astype(BF16), b.astype(BF16), preferred_element_type=F32)
    return jnp.dot(a, b, precision=prec, preferred_element_type=F32)


def _dot_nt(a, b, prec=None):
    if prec is None:
        a, b = a.astype(BF16), b.astype(BF16)
    return lax.dot_general(a, b, (((1,), (1,)), ((), ())), precision=prec, preferred_element_type=F32)


def _dot_tn(a, b, prec=None):
    if prec is None:
        a, b = a.astype(BF16), b.astype(BF16)
    return lax.dot_general(a, b, (((0,), (0,)), ((), ())), precision=prec, preferred_element_type=F32)


def _gdr_chunk_terms(k, beta, g):
    c = GDR_CHUNK
    row = lax.broadcasted_iota(jnp.int32, (c, c), 0)
    col = lax.broadcasted_iota(jnp.int32, (c, c), 1)
    causal, strict = row >= col, row > col
    gcum = _dot(causal.astype(F32), jnp.broadcast_to(g, (c, c)), HIGHEST)
    diff = gcum - gcum.T
    decay = jnp.where(causal, jnp.exp(jnp.where(causal, diff, 0.0)), 0.0)
    kb = k * beta
    kk = _dot_nt(kb, k)
    return row, col, causal, strict, gcum, decay, kb, kk


def _unit_lower_inverse(a):
    c = a.shape[0]
    eye = (lax.broadcasted_iota(jnp.int32, (c, c), 0) == lax.broadcasted_iota(jnp.int32, (c, c), 1)).astype(F32)
    p = -a
    inv = eye + p
    step = 1
    while 2 * step < c:
        p = _dot(p, p, HIGH)
        inv = inv + _dot(inv, p, HIGH)
        step *= 2
    return inv


def _head_gates(gates, head):
    lane = lax.broadcasted_iota(jnp.int32, gates.shape, 1)
    beta = jnp.sum(jnp.where(lane == head, gates, 0.0), axis=1, keepdims=True)
    g = jnp.sum(jnp.where(lane == head + N_HEADS_A, gates, 0.0), axis=1, keepdims=True)
    return beta, g


def _gdr_fwd(q, k, v, gates, name, comm=None):
    t = q.shape[0]
    c = GDR_CHUNK
    n = t // c

    hps = GDR_HEADS_PER_STEP

    def one_head(hh, q_ref, k_ref, v_ref, gb_ref, o_ref, tm_ref, s_ref, state):
        cols = slice(hh * HEAD_A, (hh + 1) * HEAD_A)
        qv, kv, vv = q_ref[:, cols], k_ref[:, cols], v_ref[:, cols]
        beta, g = _head_gates(gb_ref[...], pl.program_id(0) * hps + hh)
        row, col, causal, strict, gcum, decay, kb, kk = _gdr_chunk_terms(kv, beta, g)
        tm = _unit_lower_inverse(jnp.where(strict, kk * decay, 0.0))
        e = jnp.exp(gcum)
        u = _dot(tm, vv * beta, HIGH)
        w = _dot(tm, kb * e, HIGH)
        p = jnp.where(causal, _dot_nt(qv, kv) * decay, 0.0)
        s = state[hh]
        s_ref[hh, 0] = s
        tm_ref[hh, 0] = tm
        vn = u - _dot(w, s)
        o_ref[:, cols] = _dot(qv * e, s) + _dot(p, vn)
        glast = gcum[c - 1:c, :]
        state[hh] = s * jnp.exp(glast) + _dot_tn(kv * jnp.exp(glast - gcum), vn)

    def body(*refs):
        state = refs[-1]

        @pl.when(pl.program_id(1) == 0)
        def _():
            state[...] = jnp.zeros_like(state)

        for hh in range(hps):
            one_head(hh, *refs)

    blk = pl.BlockSpec((c, hps * HEAD_A), lambda h, i: (i, h))
    mat = pl.BlockSpec((hps, 1, c, c), lambda h, i: (h, i, 0, 0))
    return _call(
        body, name=name, grid=(N_HEADS_A // hps, n),
        in_specs=[blk, blk, blk, pl.BlockSpec((c, LANE), lambda h, i: (i, 0))],
        out_specs=[blk, mat, mat],
        out_shape=[jax.ShapeDtypeStruct((t, WIDTH_A), F32), jax.ShapeDtypeStruct((N_HEADS_A, n, c, c), F32),
                   jax.ShapeDtypeStruct((N_HEADS_A, n, HEAD_A, HEAD_A), F32)],
        scratch_shapes=[pltpu.VMEM((hps, HEAD_A, HEAD_A), F32)], sem=("parallel", "arbitrary"),
        args=(q, k, v, gates), comm=comm)


def _gdr_bwd(q, k, v, gates, tm_all, s_all, do, name, comm=None):
    t = q.shape[0]
    c = GDR_CHUNK
    n = t // c

    hps = GDR_HEADS_PER_STEP

    def one_head(hh, q_ref, k_ref, v_ref, gb_ref, tm_ref, s_ref, do_ref, dq_ref, dk_ref, dv_ref, dgb_ref, dstate):
        cols = slice(hh * HEAD_A, (hh + 1) * HEAD_A)
        qv, kv, vv, dov = q_ref[:, cols], k_ref[:, cols], v_ref[:, cols], do_ref[:, cols]
        head = pl.program_id(0) * hps + hh
        beta, g = _head_gates(gb_ref[...], head)
        tm, s, dsp = tm_ref[hh, 0], s_ref[hh, 0], dstate[hh]
        row, col, causal, strict, gcum, decay, kb, kk = _gdr_chunk_terms(kv, beta, g)
        e = jnp.exp(gcum)
        vb, kbe = vv * beta, kb * e
        u = _dot(tm, vb, HIGH)
        w = _dot(tm, kbe, HIGH)
        qk = _dot_nt(qv, kv)
        p = jnp.where(causal, qk * decay, 0.0)
        vn = u - _dot(w, s)
        glast = gcum[c - 1:c, :]
        el = jnp.exp(glast)
        f = jnp.exp(glast - gcum)
        kd = kv * f
        qe = qv * e

        dvn = _dot_tn(p, dov) + _dot(kd, dsp)
        dglast = el[:, 0:1] * jnp.sum(s * dsp, keepdims=True)
        dkd = _dot_nt(vn, dsp)
        dk = dkd * f
        df = jnp.sum(dkd * kv, axis=1, keepdims=True) * f[:, 0:1]
        dglast = dglast + jnp.sum(df, keepdims=True)
        dgc = -df
        dp = jnp.where(causal, _dot_nt(dov, vn), 0.0)
        dqe = _dot_nt(dov, s)
        dq = dqe * e
        de = jnp.sum(dqe * qv, axis=1, keepdims=True)
        dstate[hh] = dsp * el + _dot_tn(qe, dov) - _dot_tn(w, dvn)
        dw = -_dot_nt(dvn, s)
        dvb = _dot_tn(tm, dvn, HIGH)
        dkbe = _dot_tn(tm, dw, HIGH)
        da = -jnp.where(strict, _dot_nt(dvb, u) + _dot_nt(dkbe, w), 0.0)
        dkk = da * decay
        dqk = dp * decay
        dd = da * kk + dp * qk
        dq = dq + _dot(dqk, kv)
        dk = dk + _dot_tn(dqk, qv)
        dkb = _dot(dkk, kv) + dkbe * e
        dk = dk + _dot_tn(dkk, kb)
        de = de + jnp.sum(dkbe * kb, axis=1, keepdims=True)
        dk = dk + dkb * beta
        dbeta = jnp.sum(dkb * kv, axis=1, keepdims=True) + jnp.sum(dvb * vv, axis=1, keepdims=True)
        m = dd * decay
        dgc = dgc + jnp.sum(m, axis=1, keepdims=True) - jnp.sum(m.T, axis=1, keepdims=True)
        dgc = dgc + de * e[:, 0:1]
        dgc = dgc + jnp.where(row[:, 0:1] == c - 1, dglast, 0.0)
        dg = _dot((row <= col).astype(F32), jnp.broadcast_to(dgc, (c, c)), HIGHEST)
        dq_ref[:, cols] = dq
        dk_ref[:, cols] = dk
        dv_ref[:, cols] = dvb * beta
        lane = lax.broadcasted_iota(jnp.int32, (c, LANE), 1)
        dgb_ref[hh] = jnp.where(lane == head, dbeta, jnp.where(lane == head + N_HEADS_A, dg, 0.0))

    def body(*refs):
        dstate = refs[-1]

        @pl.when(pl.program_id(1) == 0)
        def _():
            dstate[...] = jnp.zeros_like(dstate)

        for hh in range(hps):
            one_head(hh, *refs)

    blk = pl.BlockSpec((c, hps * HEAD_A), lambda h, i: (n - 1 - i, h))
    mat = pl.BlockSpec((hps, 1, c, c), lambda h, i: (h, n - 1 - i, 0, 0))
    return _call(
        body, name=name, grid=(N_HEADS_A // hps, n),
        in_specs=[blk, blk, blk, pl.BlockSpec((c, LANE), lambda h, i: (n - 1 - i, 0)), mat, mat, blk],
        out_specs=[blk, blk, blk, pl.BlockSpec((hps, c, LANE), lambda h, i: (h, n - 1 - i, 0))],
        out_shape=[jax.ShapeDtypeStruct((t, WIDTH_A), F32)] * 3 + [jax.ShapeDtypeStruct((N_HEADS_A, t, LANE), F32)],
        scratch_shapes=[pltpu.VMEM((hps, HEAD_A, HEAD_A), F32)], sem=("parallel", "arbitrary"),
        args=(q, k, v, gates, tm_all, s_all, do), comm=comm)


_B_NN, _B_NT, _B_TN = ((2,), (1,)), ((2,), (2,)), ((1,), (1,))


def _bdot(a, b, dims=_B_NN, prec=None):
    if prec is None:
        a, b = a.astype(BF16), b.astype(BF16)
    return lax.dot_general(a, b, (dims, ((0,), (0,))), precision=prec, preferred_element_type=F32)


def _heads_of(ref):
    return jnp.stack([ref[:, h * HEAD_A:(h + 1) * HEAD_A] for h in range(N_HEADS_A)])


def _all_head_gates(gates):
    pairs = [_head_gates(gates, h) for h in range(N_HEADS_A)]
    return jnp.stack([b for b, _ in pairs]), jnp.stack([g for _, g in pairs])


def _gdr_terms(k, beta, g):
    h, c = k.shape[0], GDR_CHUNK
    row = lax.broadcasted_iota(jnp.int32, (c, c), 0)
    col = lax.broadcasted_iota(jnp.int32, (c, c), 1)
    causal, strict = row >= col, row > col
    lower = jnp.broadcast_to(causal.astype(F32), (h, c, c))
    gcum = _bdot(lower, jnp.broadcast_to(g, (h, c, c)), prec=HIGHEST)
    diff = gcum - jnp.swapaxes(gcum, 1, 2)
    decay = jnp.where(causal, jnp.exp(jnp.where(causal, diff, 0.0)), 0.0)
    kb = k * beta
    return row, col, causal, strict, gcum, decay, kb, _bdot(kb, k, _B_NT)


def _unit_lower_inverses(a):
    c = a.shape[1]
    eye = (lax.broadcasted_iota(jnp.int32, (c, c), 0) == lax.broadcasted_iota(jnp.int32, (c, c), 1)).astype(F32)
    p = -a
    inv = eye + p
    step = 1
    while 2 * step < c:
        p = _bdot(p, p, prec=HIGH)
        inv = inv + _bdot(inv, p, prec=HIGH)
        step *= 2
    return inv


def _gdr_fwd(q, k, v, gates, name, comm=None):
    t = q.shape[0]
    c, nh = GDR_CHUNK, N_HEADS_A
    n = t // c

    def body(q_ref, k_ref, v_ref, gb_ref, o_ref, tm_ref, s_ref, state):
        @pl.when(pl.program_id(0) == 0)
        def _():
            state[...] = jnp.zeros_like(state)

        qv, kv, vv = _heads_of(q_ref), _heads_of(k_ref), _heads_of(v_ref)
        beta, g = _all_head_gates(gb_ref[...])
        row, col, causal, strict, gcum, decay, kb, kk = _gdr_terms(kv, beta, g)
        tm = _unit_lower_inverses(jnp.where(strict, kk * decay, 0.0))
        e = jnp.exp(gcum)
        u = _bdot(tm, vv * beta, prec=HIGH)
        w = _bdot(tm, kb * e, prec=HIGH)
        p = jnp.where(causal, _bdot(qv, kv, _B_NT) * decay, 0.0)
        s = state[...]
        s_ref[:, 0] = s
        tm_ref[:, 0] = tm
        vn = u - _bdot(w, s)
        o = _bdot(qv * e, s) + _bdot(p, vn)
        for h in range(nh):
            o_ref[:, h * HEAD_A:(h + 1) * HEAD_A] = o[h]
        glast = gcum[:, c - 1:c, :]
        state[...] = s * jnp.exp(glast) + _bdot(kv * jnp.exp(glast - gcum), vn, _B_TN)

    blk = pl.BlockSpec((c, WIDTH_A), lambda i: (i, 0))
    mat = pl.BlockSpec((nh, 1, c, c), lambda i: (0, i, 0, 0))
    return _call(
        body, name=name, grid=(n,), in_specs=[blk, blk, blk, pl.BlockSpec((c, LANE), lambda i: (i, 0))],
        out_specs=[blk, mat, mat],
        out_shape=[jax.ShapeDtypeStruct((t, WIDTH_A), F32), jax.ShapeDtypeStruct((nh, n, c, c), F32),
                   jax.ShapeDtypeStruct((nh, n, HEAD_A, HEAD_A), F32)],
        scratch_shapes=[pltpu.VMEM((nh, HEAD_A, HEAD_A), F32)], sem=("arbitrary",),
        args=(q, k, v, gates), comm=comm)


def _gdr_bwd(q, k, v, gates, tm_all, s_all, do, name, comm=None):
    t = q.shape[0]
    c, nh = GDR_CHUNK, N_HEADS_A
    n = t // c

    def body(q_ref, k_ref, v_ref, gb_ref, tm_ref, s_ref, do_ref, dq_ref, dk_ref, dv_ref, dgb_ref, dstate):
        @pl.when(pl.program_id(0) == 0)
        def _():
            dstate[...] = jnp.zeros_like(dstate)

        qv, kv, vv, dov = _heads_of(q_ref), _heads_of(k_ref), _heads_of(v_ref), _heads_of(do_ref)
        beta, g = _all_head_gates(gb_ref[...])
        tm, s, dsp = tm_ref[:, 0], s_ref[:, 0], dstate[...]
        row, col, causal, strict, gcum, decay, kb, kk = _gdr_terms(kv, beta, g)
        rowsum = lambda x: jnp.sum(x, axis=2, keepdims=True)
        e = jnp.exp(gcum)
        vb, kbe = vv * beta, kb * e
        u = _bdot(tm, vb, prec=HIGH)
        w = _bdot(tm, kbe, prec=HIGH)
        qk = _bdot(qv, kv, _B_NT)
        p = jnp.where(causal, qk * decay, 0.0)
        vn = u - _bdot(w, s)
        glast = gcum[:, c - 1:c, :]
        el = jnp.exp(glast)
        f = jnp.exp(glast - gcum)
        kd = kv * f
        qe = qv * e

        dvn = _bdot(p, dov, _B_TN) + _bdot(kd, dsp)
        dglast = el[:, :, 0:1] * jnp.sum(s * dsp, axis=(1, 2), keepdims=True)
        dkd = _bdot(vn, dsp, _B_NT)
        dk = dkd * f
        df = rowsum(dkd * kv) * f[:, :, 0:1]
        dglast = dglast + jnp.sum(df, axis=1, keepdims=True)
        dgc = -df
        dp = jnp.where(causal, _bdot(dov, vn, _B_NT), 0.0)
        dqe = _bdot(dov, s, _B_NT)
        dq = dqe * e
        de = rowsum(dqe * qv)
        dstate[...] = dsp * el + _bdot(qe, dov, _B_TN) - _bdot(w, dvn, _B_TN)
        dw = -_bdot(dvn, s, _B_NT)
        dvb = _bdot(tm, dvn, _B_TN, prec=HIGH)
        dkbe = _bdot(tm, dw, _B_TN, prec=HIGH)
        da = -jnp.where(strict, _bdot(dvb, u, _B_NT) + _bdot(dkbe, w, _B_NT), 0.0)
        dkk = da * decay
        dqk = dp * decay
        dd = da * kk + dp * qk
        dq = dq + _bdot(dqk, kv)
        dk = dk + _bdot(dqk, qv, _B_TN)
        dkb = _bdot(dkk, kv) + dkbe * e
        dk = dk + _bdot(dkk, kb, _B_TN)
        de = de + rowsum(dkbe * kb)
        dk = dk + dkb * beta
        dbeta = rowsum(dkb * kv) + rowsum(dvb * vv)
        m = dd * decay
        dgc = dgc + rowsum(m) - rowsum(jnp.swapaxes(m, 1, 2))
        dgc = dgc + de * e[:, :, 0:1]
        dgc = dgc + jnp.where(row[:, 0:1] == c - 1, dglast, 0.0)
        upper = jnp.broadcast_to((row <= col).astype(F32), (nh, c, c))
        dg = _bdot(upper, jnp.broadcast_to(dgc, (nh, c, c)), prec=HIGHEST)
        dv = dvb * beta
        for h in range(nh):
            cols = slice(h * HEAD_A, (h + 1) * HEAD_A)
            dq_ref[:, cols] = dq[h]
            dk_ref[:, cols] = dk[h]
            dv_ref[:, cols] = dv[h]
        head = lax.broadcasted_iota(jnp.int32, (nh, c, LANE), 0)
        lane = lax.broadcasted_iota(jnp.int32, (nh, c, LANE), 2)
        dgb_ref[...] = jnp.where(lane == head, dbeta, jnp.where(lane == head + nh, dg, 0.0))

    blk = pl.BlockSpec((c, WIDTH_A), lambda i: (n - 1 - i, 0))
    mat = pl.BlockSpec((nh, 1, c, c), lambda i: (0, n - 1 - i, 0, 0))
    return _call(
        body, name=name, grid=(n,),
        in_specs=[blk, blk, blk, pl.BlockSpec((c, LANE), lambda i: (n - 1 - i, 0)), mat, mat, blk],
        out_specs=[blk, blk, blk, pl.BlockSpec((nh, c, LANE), lambda i: (0, n - 1 - i, 0))],
        out_shape=[jax.ShapeDtypeStruct((t, WIDTH_A), F32)] * 3 + [jax.ShapeDtypeStruct((nh, t, LANE), F32)],
        scratch_shapes=[pltpu.VMEM((nh, HEAD_A, HEAD_A), F32)], sem=("arbitrary",),
        args=(q, k, v, gates, tm_all, s_all, do), comm=comm)


def _onorm_fwd(o, gate, g, name):
    t = o.shape[0]

    def body(o_ref, gate_ref, g_ref, y_ref):
        ov, gv = o_ref[...], gate_ref[...]
        r = lax.rsqrt(jnp.mean(ov * ov, axis=-1, keepdims=True) + RMS_EPS)
        y_ref[...] = (ov * r * g_ref[...] * gv * _sigmoid(gv)).astype(BF16)

    blk = pl.BlockSpec((t, HEAD_A), lambda j: (0, j))
    return pl.pallas_call(
        body, name=name, grid=(N_HEADS_A,), in_specs=[blk, blk, pl.BlockSpec((1, HEAD_A), lambda j: (0, 0))],
        out_specs=blk, out_shape=jax.ShapeDtypeStruct((t, WIDTH_A), BF16),
        compiler_params=_params(("parallel",)))(o, gate, g)


def _onorm_bwd(o, gate, g, dy, name):
    t = o.shape[0]

    def body(o_ref, gate_ref, g_ref, dy_ref, do_ref, dgate_ref, dg_ref):
        @pl.when(pl.program_id(0) == 0)
        def _():
            dg_ref[...] = jnp.zeros_like(dg_ref)

        ov, gv, dyv = o_ref[...], gate_ref[...], dy_ref[...].astype(F32)
        r = lax.rsqrt(jnp.mean(ov * ov, axis=-1, keepdims=True) + RMS_EPS)
        oh = ov * r
        sg, dsg = _silu_and_grad(gv)
        dgate_ref[...] = (dyv * oh * g_ref[...] * dsg).astype(BF16)
        dn = dyv * sg
        dg_ref[...] += jnp.sum(dn * oh, axis=0, keepdims=True)
        dng = dn * g_ref[...]
        do_ref[...] = r * (dng - oh * jnp.mean(dng * oh, axis=-1, keepdims=True))

    blk = pl.BlockSpec((t, HEAD_A), lambda j: (0, j))
    vec = pl.BlockSpec((1, HEAD_A), lambda j: (0, 0))
    return pl.pallas_call(
        body, name=name, grid=(N_HEADS_A,), in_specs=[blk, blk, vec, blk], out_specs=[blk, blk, vec],
        out_shape=[jax.ShapeDtypeStruct((t, WIDTH_A), F32), jax.ShapeDtypeStruct((t, WIDTH_A), BF16),
                   jax.ShapeDtypeStruct((1, HEAD_A), F32)],
        compiler_params=_params(("arbitrary",)))(o, gate, g, dy)


def _cmul(ar, ai, br, bi):
    return ar * br - ai * bi, ar * bi + ai * br


def _scan_tables(ar, ai, reverse):
    p1 = (ar, ai)
    p2 = _cmul(*p1, *p1)
    p4 = _cmul(*p2, *p2)
    p8 = _cmul(*p4, *p4)
    p3 = _cmul(*p2, *p1)
    p5 = _cmul(*p4, *p1)
    p6 = _cmul(*p4, *p2)
    p7 = _cmul(*p4, *p3)
    pows = [p1, p2, p3, p4, p5, p6, p7, p8]
    rows = lax.broadcasted_iota(jnp.int32, (8, ar.shape[1]), 0)
    tr = jnp.zeros((8, ar.shape[1]), F32)
    ti = jnp.zeros((8, ar.shape[1]), F32)
    for r in range(8):
        pw = pows[7 - r] if reverse else pows[r]
        tr = jnp.where(rows == r, pw[0], tr)
        ti = jnp.where(rows == r, pw[1], ti)
    return p1, p2, p4, p8, tr, ti


def _tile_scan(xr, xi, p1, p2, p4, reverse):
    rows = lax.broadcasted_iota(jnp.int32, xr.shape, 0)
    for s, (pr, pi) in ((1, p1), (2, p2), (4, p4)):
        if reverse:
            keep = rows < 8 - s
            sr, si = pltpu.roll(xr, 8 - s, 0), pltpu.roll(xi, 8 - s, 0)
        else:
            keep = rows >= s
            sr, si = pltpu.roll(xr, s, 0), pltpu.roll(xi, s, 0)
        sr, si = jnp.where(keep, sr, 0.0), jnp.where(keep, si, 0.0)
        mr, mi = _cmul(pr, pi, sr, si)
        xr, xi = xr + mr, xi + mi
    return xr, xi


def _s5_scan_fwd(bu, a, name, tb=512, comm=None):
    t = bu.shape[0]
    cb = SCAN_CB
    nt = t // tb

    def body(b_ref, a_ref, x_ref, carry):
        @pl.when(pl.program_id(1) == 0)
        def _():
            carry[...] = jnp.zeros_like(carry)

        ar, ai = a_ref[:, 0:cb], a_ref[:, cb:2 * cb]
        p1, p2, p4, p8, tr, ti = _scan_tables(ar, ai, False)

        def step(j, c):
            cr, ci = c
            i = pl.multiple_of(j * 8, 8)
            xr, xi = _tile_scan(b_ref[pl.ds(i, 8), 0:cb], b_ref[pl.ds(i, 8), cb:2 * cb], p1, p2, p4, False)
            mr, mi = _cmul(tr, ti, cr, ci)
            xr, xi = xr + mr, xi + mi
            x_ref[pl.ds(i, 8), 0:cb] = xr
            x_ref[pl.ds(i, 8), cb:2 * cb] = xi
            return xr[7:8, :], xi[7:8, :]

        cr, ci = lax.fori_loop(0, tb // 8, step, (carry[0:1, :], carry[1:2, :]), unroll=2)
        carry[0:1, :] = cr
        carry[1:2, :] = ci

    blk = pl.BlockSpec((tb, 2 * cb), lambda j, i: (i, j))
    return _call(
        body, name=name, grid=(SSM_CH // cb, nt),
        in_specs=[blk, pl.BlockSpec((1, 2 * cb), lambda j, i: (0, j))], out_specs=blk,
        out_shape=jax.ShapeDtypeStruct((t, 2 * SSM_CH), F32), scratch_shapes=[pltpu.VMEM((8, cb), F32)],
        sem=("parallel", "arbitrary"), args=(bu, a), comm=comm)


def _s5_scan_bwd(dx, x, a, name, tb=512, comm=None):
    t = dx.shape[0]
    cb = SCAN_CB
    nt = t // tb
    nj = tb // 8

    def body(d_ref, x_ref, xp_ref, a_ref, l_ref, da_ref, carry, acc):
        tblk = pl.program_id(1)

        @pl.when(tblk == 0)
        def _():
            carry[...] = jnp.zeros_like(carry)
            acc[...] = jnp.zeros_like(acc)

        ar, ai = a_ref[:, 0:cb], a_ref[:, cb:2 * cb]
        p1, p2, p4, p8, tr, ti = _scan_tables(ar, -ai, True)
        rows = lax.broadcasted_iota(jnp.int32, (8, cb), 0)

        def step(jj, c):
            cr, ci, sr_acc, si_acc = c
            j = nj - 1 - jj
            i = pl.multiple_of(j * 8, 8)
            lr, li = _tile_scan(d_ref[pl.ds(i, 8), 0:cb], d_ref[pl.ds(i, 8), cb:2 * cb], p1, p2, p4, True)
            mr, mi = _cmul(tr, ti, cr, ci)
            lr, li = lr + mr, li + mi
            l_ref[pl.ds(i, 8), 0:cb] = lr
            l_ref[pl.ds(i, 8), cb:2 * cb] = li
            ip = pl.multiple_of(jnp.maximum(j - 1, 0) * 8, 8)
            prev_r = jnp.where(j > 0, x_ref[pl.ds(ip, 8), 0:cb], xp_ref[:, 0:cb])
            prev_i = jnp.where(j > 0, x_ref[pl.ds(ip, 8), cb:2 * cb], xp_ref[:, cb:2 * cb])
            edge = jnp.where(jnp.logical_and(j == 0, tblk == nt - 1), 0.0, 1.0)
            xs_r = jnp.where(rows == 0, pltpu.roll(prev_r, 1, 0) * edge, pltpu.roll(x_ref[pl.ds(i, 8), 0:cb], 1, 0))
            xs_i = jnp.where(rows == 0, pltpu.roll(prev_i, 1, 0) * edge, pltpu.roll(x_ref[pl.ds(i, 8), cb:2 * cb], 1, 0))
            sr_acc = sr_acc + lr * xs_r + li * xs_i
            si_acc = si_acc + li * xs_r - lr * xs_i
            return lr[0:1, :], li[0:1, :], sr_acc, si_acc

        cr, ci, sr_acc, si_acc = lax.fori_loop(
            0, nj, step, (carry[0:1, :], carry[1:2, :], acc[:, 0:cb], acc[:, cb:2 * cb]))
        carry[0:1, :] = cr
        carry[1:2, :] = ci
        acc[:, 0:cb] = sr_acc
        acc[:, cb:2 * cb] = si_acc

        @pl.when(tblk == nt - 1)
        def _():
            da_ref[...] = jnp.sum(acc[...], axis=0, keepdims=True)

    blk = pl.BlockSpec((tb, 2 * cb), lambda j, i: (nt - 1 - i, j))
    prev = pl.BlockSpec((8, 2 * cb), lambda j, i: (jnp.maximum((nt - 1 - i) * (tb // 8) - 1, 0), j))
    vec = pl.BlockSpec((1, 2 * cb), lambda j, i: (0, j))
    return _call(
        body, name=name, grid=(SSM_CH // cb, nt), in_specs=[blk, blk, prev, vec], out_specs=[blk, vec],
        out_shape=[jax.ShapeDtypeStruct((t, 2 * SSM_CH), F32), jax.ShapeDtypeStruct((1, 2 * SSM_CH), F32)],
        scratch_shapes=[pltpu.VMEM((8, cb), F32), pltpu.VMEM((8, 2 * cb), F32)],
        sem=("parallel", "arbitrary"), args=(dx, x, x, a), comm=comm)


def _glu_fwd(yc, u, dvec, wg, bg, name, tr=256):
    t = yc.shape[0]

    def body(yc_ref, u_ref, d_ref, w_ref, b_ref, yl_ref, yb_ref):
        yl = yc_ref[...] + d_ref[...] * u_ref[...]
        yl_ref[...] = yl
        yg, _ = _gelu_and_grad(yl)
        z = jnp.dot(yg.astype(BF16), w_ref[...], preferred_element_type=F32) + b_ref[...]
        yb_ref[...] = (yg * _sigmoid(z)).astype(BF16)

    blk = pl.BlockSpec((tr, SSM_WIDTH), lambda i: (i, 0))
    vec = pl.BlockSpec((1, SSM_WIDTH), lambda i: (0, 0))
    return pl.pallas_call(
        body, name=name, grid=(t // tr,),
        in_specs=[blk, blk, vec, pl.BlockSpec((SSM_WIDTH, SSM_WIDTH), lambda i: (0, 0)), vec],
        out_specs=[blk, blk],
        out_shape=[jax.ShapeDtypeStruct((t, SSM_WIDTH), F32), jax.ShapeDtypeStruct((t, SSM_WIDTH), BF16)],
        compiler_params=_params(("parallel",)))(yc, u, dvec, wg, bg)


def _glu_bwd(yl, u, dvec, wg, bg, dyb, name, tr=256):
    t = yl.shape[0]

    def body(yl_ref, u_ref, d_ref, w_ref, b_ref, dy_ref, dyl_ref, du_ref, dw_ref, db_ref, dd_ref):
        @pl.when(pl.program_id(0) == 0)
        def _():
            dw_ref[...] = jnp.zeros_like(dw_ref)
            db_ref[...] = jnp.zeros_like(db_ref)
            dd_ref[...] = jnp.zeros_like(dd_ref)

        ylv, dyv, wv = yl_ref[...], dy_ref[...].astype(F32), w_ref[...]
        yg, dgelu = _gelu_and_grad(ylv)
        ygb = yg.astype(BF16)
        z = jnp.dot(ygb, wv, preferred_element_type=F32) + b_ref[...]
        sg = _sigmoid(z)
        dz = dyv * yg * sg * (1.0 - sg)
        dzb = dz.astype(BF16)
        dyg = dyv * sg + lax.dot_general(dzb, wv, (((1,), (1,)), ((), ())), preferred_element_type=F32)
        dyl = dyg * dgelu
        dyl_ref[...] = dyl.astype(BF16)
        du_ref[...] = dyl * d_ref[...]
        dw_ref[...] += lax.dot_general(ygb, dzb, (((0,), (0,)), ((), ())), preferred_element_type=F32)
        db_ref[...] += jnp.sum(dz, axis=0, keepdims=True)
        dd_ref[...] += jnp.sum(dyl * u_ref[...], axis=0, keepdims=True)

    blk = pl.BlockSpec((tr, SSM_WIDTH), lambda i: (i, 0))
    vec = pl.BlockSpec((1, SSM_WIDTH), lambda i: (0, 0))
    wsp = pl.BlockSpec((SSM_WIDTH, SSM_WIDTH), lambda i: (0, 0))
    return pl.pallas_call(
        body, name=name, grid=(t // tr,), in_specs=[blk, blk, vec, wsp, vec, blk],
        out_specs=[blk, blk, wsp, vec, vec],
        out_shape=[jax.ShapeDtypeStruct((t, SSM_WIDTH), BF16), jax.ShapeDtypeStruct((t, SSM_WIDTH), F32),
                   jax.ShapeDtypeStruct((SSM_WIDTH, SSM_WIDTH), F32), jax.ShapeDtypeStruct((1, SSM_WIDTH), F32),
                   jax.ShapeDtypeStruct((1, SSM_WIDTH), F32)],
        compiler_params=_params(("arbitrary",)))(yl, u, dvec, wg, bg, dyb)


def _mesh_pos():
    return lax.axis_index("x"), lax.axis_index("y"), lax.axis_index("c")


def _device_index():
    x, y, c = _mesh_pos()
    return 4 * x + 2 * y + c


def _gather_comm(arrays):
    na = len(arrays)

    def own_copy(ins, outs, sems, ai):
        return pltpu.make_async_copy(ins[ai], outs[ai].at[_device_index()], sems[2].at[ai])

    def ctx(ins, outs, sems):
        send_sems, recv_sems = sems[:2]
        x, y, c = _mesh_pos()
        chips = [(1 - x, y), (x, 1 - y), (1 - x, 1 - y)]

        def copy(ai, kk, block, to, own=False):
            slot = outs[ai].at[4 * block[0] + 2 * block[1] + block[2]]
            return pltpu.make_async_remote_copy(
                src_ref=ins[ai] if own else slot, dst_ref=slot, send_sem=send_sems.at[ai, kk],
                recv_sem=recv_sems.at[ai, kk], device_id=to, device_id_type=MESH)

        return (x, y, c), (x, y, 1 - c), chips, c, copy

    def start(ins, outs, sems):
        me, sibling, chips, c, copy = ctx(ins, outs, sems)
        for ai in range(na):
            copy(ai, 0, me, sibling, own=True).start()
            for j, chip in enumerate(chips):
                copy(ai, 1 + j, me, (*chip, c), own=True).start()
        for ai in range(na):
            own_copy(ins, outs, sems, ai).start()

    def mid(ins, outs, sems):
        me, sibling, chips, c, copy = ctx(ins, outs, sems)
        for ai in range(na):
            for j, chip in enumerate(chips):
                copy(ai, 1 + j, (*chip, c), me).wait_recv()
                copy(ai, 4 + j, (*chip, c), sibling).start()

    def end(ins, outs, sems):
        me, sibling, chips, c, copy = ctx(ins, outs, sems)
        for ai in range(na):
            copy(ai, 0, sibling, me).wait_recv()
            copy(ai, 0, me, sibling, own=True).wait_send()
            for j, chip in enumerate(chips):
                copy(ai, 4 + j, (*chip, 1 - c), me).wait_recv()
                copy(ai, 1 + j, me, (*chip, c), own=True).wait_send()
                copy(ai, 4 + j, (*chip, c), sibling).wait_send()
            own_copy(ins, outs, sems, ai).wait()

    return Comm(arrays, [jax.ShapeDtypeStruct((N_DEV,) + a.shape, a.dtype) for a in arrays],
                [pltpu.SemaphoreType.DMA((na, 7)), pltpu.SemaphoreType.DMA((na, 7)), pltpu.SemaphoreType.DMA((na,))],
                start, end, mid)


def _swap_comm(arrays):
    na = len(arrays)
    offs = np.concatenate([[0], np.cumsum([a.shape[1] for a in arrays])]).astype(int)

    def copies(ins, outs, sems):
        x, y, c = _mesh_pos()
        return [pltpu.make_async_remote_copy(
            src_ref=ins[ai].at[2 * k + 1 - c], dst_ref=outs[0].at[k, pl.ds(int(offs[ai]), arrays[ai].shape[1])],
            send_sem=sems[0].at[ai, k], recv_sem=sems[1].at[ai, k], device_id=(x, y, 1 - c), device_id_type=MESH)
            for ai in range(na) for k in range(4)]

    def start(ins, outs, sems):
        for cp in copies(ins, outs, sems):
            cp.start()

    def end(ins, outs, sems):
        for cp in copies(ins, outs, sems):
            cp.wait()

    return Comm(arrays, [jax.ShapeDtypeStruct((4, int(offs[-1]), PACK_COLS), arrays[0].dtype)],
                [pltpu.SemaphoreType.DMA((na, 4)), pltpu.SemaphoreType.DMA((na, 4))], start, end)


def _chips_comm(send):
    def copies(ins, outs, sems):
        x, y, c = _mesh_pos()
        chips = [(1 - x, y), (x, 1 - y), (1 - x, 1 - y)]
        return [pltpu.make_async_remote_copy(
            src_ref=ins[0].at[2 * cx + cy], dst_ref=outs[0].at[j], send_sem=sems[0].at[j], recv_sem=sems[1].at[j],
            device_id=(cx, cy, c), device_id_type=MESH) for j, (cx, cy) in enumerate(chips)]

    def start(ins, outs, sems):
        for cp in copies(ins, outs, sems):
            cp.start()

    def end(ins, outs, sems):
        for cp in copies(ins, outs, sems):
            cp.wait()

    return Comm([send], [jax.ShapeDtypeStruct((3,) + send.shape[1:], send.dtype)],
                [pltpu.SemaphoreType.DMA((3,)), pltpu.SemaphoreType.DMA((3,))], start, end)


def _all_gather(arrays, name):
    na = len(arrays)

    def body(*refs):
        ins, outs = refs[:na], refs[na:2 * na]
        send_sems, recv_sems, local_sems = refs[2 * na:]
        x, y, c = _mesh_pos()
        me, sibling = (x, y, c), (x, y, 1 - c)
        chips = [(1 - x, y), (x, 1 - y), (1 - x, 1 - y)]
        waits = []
        for ai in range(na):
            in_ref, out_ref = ins[ai], outs[ai]

            def slot(px, py, pc, out_ref=out_ref):
                return out_ref.at[4 * px + 2 * py + pc]

            def copy(kk, block, to, src=None, ai=ai, slot=slot):
                return pltpu.make_async_remote_copy(
                    src_ref=slot(*block) if src is None else src, dst_ref=slot(*block),
                    send_sem=send_sems.at[ai, kk], recv_sem=recv_sems.at[ai, kk], device_id=to, device_id_type=MESH)

            mine = pltpu.make_async_copy(in_ref, slot(*me), local_sems.at[ai])
            mine.start()
            first = [copy(0, me, sibling, src=in_ref)]
            first += [copy(1 + j, me, (*chip, c), src=in_ref) for j, chip in enumerate(chips)]
            for cp in first:
                cp.start()
            waits.append((copy, mine, first))
        sends = []
        for ai in range(na):
            copy, mine, first = waits[ai]
            passed = [copy(4 + j, (*chip, c), sibling) for j, chip in enumerate(chips)]
            for j, chip in enumerate(chips):
                copy(1 + j, (*chip, c), me).wait_recv()
                passed[j].start()
            sends.append(passed)
        for ai in range(na):
            copy, mine, first = waits[ai]
            copy(0, sibling, me).wait_recv()
            for j, chip in enumerate(chips):
                copy(4 + j, (*chip, 1 - c), me).wait_recv()
            for cp in first + sends[ai]:
                cp.wait_send()
            mine.wait()

    any_spec = pl.BlockSpec(memory_space=pl.ANY)
    return pl.pallas_call(
        body, name=name, in_specs=[any_spec] * na, out_specs=[any_spec] * na,
        out_shape=[jax.ShapeDtypeStruct((N_DEV,) + a.shape, a.dtype) for a in arrays],
        scratch_shapes=[pltpu.SemaphoreType.DMA((na, 7)), pltpu.SemaphoreType.DMA((na, 7)),
                        pltpu.SemaphoreType.DMA((na,))],
        compiler_params=pltpu.CompilerParams(has_side_effects=True))(*arrays)


def _swap_sibling(arrays, name):
    na = len(arrays)
    offs = np.concatenate([[0], np.cumsum([a.shape[1] for a in arrays])]).astype(int)
    rows = int(offs[-1])

    def body(*refs):
        ins, recv_ref = refs[:na], refs[na]
        send_sems, recv_sems = refs[na + 1:]
        x, y, c = _mesh_pos()
        started = []
        for ai in range(na):
            span = pl.ds(int(offs[ai]), arrays[ai].shape[1])
            for k in range(4):
                remote = pltpu.make_async_remote_copy(
                    src_ref=ins[ai].at[2 * k + 1 - c], dst_ref=recv_ref.at[k, span], send_sem=send_sems.at[ai, k],
                    recv_sem=recv_sems.at[ai, k], device_id=(x, y, 1 - c), device_id_type=MESH)
                remote.start()
                started.append(remote)
        for remote in started:
            remote.wait()

    any_spec = pl.BlockSpec(memory_space=pl.ANY)
    return pl.pallas_call(
        body, name=name, in_specs=[any_spec] * na, out_specs=any_spec,
        out_shape=jax.ShapeDtypeStruct((4, rows, PACK_COLS), arrays[0].dtype),
        scratch_shapes=[pltpu.SemaphoreType.DMA((na, 4)), pltpu.SemaphoreType.DMA((na, 4))])(*arrays)


def _exchange_chips(send, name):
    def body(s_ref, o_ref, send_sems, recv_sems):
        x, y, c = _mesh_pos()
        chips = [(1 - x, y), (x, 1 - y), (1 - x, 1 - y)]
        cps = [pltpu.make_async_remote_copy(
            src_ref=s_ref.at[2 * cx + cy], dst_ref=o_ref.at[j], send_sem=send_sems.at[j], recv_sem=recv_sems.at[j],
            device_id=(cx, cy, c), device_id_type=MESH) for j, (cx, cy) in enumerate(chips)]
        for cp in cps:
            cp.start()
        for cp in cps:
            cp.wait()

    any_spec = pl.BlockSpec(memory_space=pl.ANY)
    return pl.pallas_call(
        body, name=name, in_specs=[any_spec], out_specs=any_spec,
        out_shape=jax.ShapeDtypeStruct((3,) + send.shape[1:], send.dtype),
        scratch_shapes=[pltpu.SemaphoreType.DMA((3,)), pltpu.SemaphoreType.DMA((3,))])(send)


def _pair_sum(keep, recv, name, tr=464):
    nchip, rows, cols = keep.shape

    def body(g_ref, r_ref, o_ref):
        o_ref[...] = (g_ref[...].astype(F32) + r_ref[...].astype(F32)).astype(BF16)

    blk = pl.BlockSpec((1, tr, cols), lambda k, i: (k, i, 0))
    return pl.pallas_call(
        body, name=name, grid=(nchip, rows // tr), in_specs=[blk, blk], out_specs=blk,
        out_shape=jax.ShapeDtypeStruct((nchip, rows, cols), BF16),
        compiler_params=_params(("parallel", "parallel")))(keep, recv)


def _pair_sum_pieces(pieces, recv, name, tr):
    _, rows, cols = pieces.shape
    core = lax.axis_index("c").astype(jnp.int32).reshape(1)

    def body(c_ref, g_ref, r_ref, o_ref):
        del c_ref
        o_ref[...] = (g_ref[...].astype(F32) + r_ref[...].astype(F32)).astype(BF16)

    grid_spec = pltpu.PrefetchScalarGridSpec(
        num_scalar_prefetch=1, grid=(4, rows // tr),
        in_specs=[pl.BlockSpec((1, tr, cols), lambda k, i, c_ref: (2 * k + c_ref[0], i, 0)),
                  pl.BlockSpec((1, tr, cols), lambda k, i, c_ref: (k, i, 0))],
        out_specs=pl.BlockSpec((1, tr, cols), lambda k, i, c_ref: (k, i, 0)))
    return pl.pallas_call(
        body, name=name, grid_spec=grid_spec, out_shape=jax.ShapeDtypeStruct((4, rows, cols), BF16),
        compiler_params=_params(("parallel", "parallel")))(core, pieces, recv)


def _chip_sum(own, others, name, tr=464):
    _, rows, cols = own.shape
    chip = (2 * lax.axis_index("x") + lax.axis_index("y")).astype(jnp.int32).reshape(1)

    def body(chip_ref, own_ref, oth_ref, o_ref):
        del chip_ref
        acc = own_ref[0].astype(F32)
        for j in range(3):
            acc = acc + oth_ref[j].astype(F32)
        o_ref[...] = acc

    grid_spec = pltpu.PrefetchScalarGridSpec(
        num_scalar_prefetch=1, grid=(rows // tr,),
        in_specs=[pl.BlockSpec((1, tr, cols), lambda i, chip_ref: (chip_ref[0], i, 0)),
                  pl.BlockSpec((3, tr, cols), lambda i, chip_ref: (0, i, 0))],
        out_specs=pl.BlockSpec((tr, cols), lambda i, chip_ref: (i, 0)))
    return pl.pallas_call(
        body, name=name, grid_spec=grid_spec, out_shape=jax.ShapeDtypeStruct((rows, cols), F32),
        compiler_params=_params(("parallel",)))(chip, own, others)


def _sum_leading(parts, name, tr=464):
    nparts, rows, cols = parts.shape
    tr = tr if rows % tr == 0 else rows

    def body(p_ref, o_ref):
        acc = p_ref[0].astype(F32)
        for i in range(1, nparts):
            acc = acc + p_ref[i].astype(F32)
        o_ref[...] = acc

    return pl.pallas_call(
        body, name=name, grid=(rows // tr,),
        in_specs=[pl.BlockSpec((nparts, tr, cols), lambda i: (0, i, 0))],
        out_specs=pl.BlockSpec((tr, cols), lambda i: (i, 0)), out_shape=jax.ShapeDtypeStruct((rows, cols), F32),
        compiler_params=_params(("parallel",)))(parts)


def _adamw(w, g, m, v, name, comm=None):
    shape = w.shape
    cols = shape[-1]
    lead = shape[0] if len(shape) >= 3 else 1
    rows = int(np.prod(shape[:-1])) // lead if len(shape) > 1 else 1
    w2, g2, m2, v2 = (a.reshape(lead, rows, cols) for a in (w, g, m, v))
    tr = rows
    for cand in (512, 256, 128, 64, 32, 16, 8):
        if rows % cand == 0 and rows > cand:
            tr = cand
            break
    bc1, bc2 = 1.0 - ADAM_B1 ** ADAM_STEP, 1.0 - ADAM_B2 ** ADAM_STEP

    def body(w_ref, g_ref, m_ref, v_ref, d_ref, nm_ref, nv_ref):
        gv = g_ref[...]
        nm = ADAM_B1 * m_ref[...] + (1.0 - ADAM_B1) * gv
        nv = ADAM_B2 * v_ref[...] + (1.0 - ADAM_B2) * (gv * gv)
        nm_ref[...] = nm
        nv_ref[...] = nv
        d_ref[...] = -ADAM_LR * ((nm / bc1) / (jnp.sqrt(nv / bc2) + ADAM_EPS) + ADAM_WD * w_ref[...])

    blk = pl.BlockSpec((1, tr, cols), lambda l, i: (l, i, 0))
    res = _call(body, name=name, grid=(lead, rows // tr), in_specs=[blk] * 4, out_specs=[blk] * 3,
                out_shape=[jax.ShapeDtypeStruct((lead, rows, cols), F32)] * 3, sem=("parallel", "parallel"),
                args=(w2, g2, m2, v2), comm=comm)
    outs, couts = res if comm is not None else (res, None)
    outs = tuple(o.reshape(shape) for o in outs)
    return outs if comm is None else (outs, couts)


WEIGHT_NAMES = ['norm_mix_g', 'norm_xa_g', 'norm_ffn_g', 'norm_mem_g', 'norm_final_g', 'w_in_ab', 'conv_qkv_a',
                'a_log_a', 'dt_bias_a', 'onorm_g_a', 'ssm_lambda_re', 'ssm_lambda_im', 'ssm_b_re', 'ssm_b_im',
                'ssm_c_re', 'ssm_c_im', 'ssm_d', 'ssm_log_dt', 'w_glu_b', 'b_glu_b', 'w_out_ab', 'pool_w',
                'pool_scale', 'xa_wq', 'xa_wkv', 'xa_wo', 'ffn_w_up', 'ffn_conv', 'ffn_w_down']
BIG_SHARDED = {'w_in_ab': ((1, 1024, 2568), 2), 'w_glu_b': ((1, 512, 512), 1), 'w_out_ab': ((1, 1024, 1024), 1),
               'pool_w': ((1, 4, 256, 256), 2), 'xa_wq': ((2, 1024, 1024), 1), 'xa_wkv': ((2, 1024, 2048), 2),
               'xa_wo': ((2, 1024, 1024), 1), 'ffn_w_up': ((2, 1024, 5632), 2), 'ffn_w_down': ((2, 2816, 1024), 1)}
SMALL_SHARDED = {'conv_qkv_a': ((1, 4, 1536), 2), 'pool_scale': ((1, 1024), 1), 'ffn_conv': ((2, 3, 5632), 2)}
REPLICATED = {'norm_mix_g': (2, 1024), 'norm_xa_g': (2, 1024), 'norm_ffn_g': (2, 1024), 'norm_mem_g': (1024,),
              'norm_final_g': (1024,), 'a_log_a': (1, 4), 'dt_bias_a': (1, 4), 'onorm_g_a': (1, 128),
              'ssm_lambda_re': (1, 32, 64), 'ssm_lambda_im': (1, 32, 64), 'ssm_b_re': (1, 32, 64, 16),
              'ssm_b_im': (1, 32, 64, 16), 'ssm_c_re': (1, 32, 16, 64), 'ssm_c_im': (1, 32, 16, 64),
              'ssm_d': (1, 32, 16), 'ssm_log_dt': (1, 32), 'b_glu_b': (1, 512)}
PACK_ROW_ALIGN = 8


def _shard_shape(shape, axis):
    return tuple(s // N_DEV if i == axis else s for i, s in enumerate(shape))


def _round_up(n, m):
    return (n + m - 1) // m * m


def _pack(arrays):
    total = sum(int(np.prod(a.shape)) for a in arrays)
    padded = _round_up(total, PACK_COLS * PACK_ROW_ALIGN)
    parts = [a.astype(F32).reshape(-1) for a in arrays]
    if padded != total:
        parts.append(jnp.zeros((padded - total,), F32))
    return jnp.concatenate(parts).reshape(padded // PACK_COLS, PACK_COLS)


def _unpack(packed, shapes):
    flat, out, off = packed.reshape(-1), [], 0
    for shape in shapes:
        size = int(np.prod(shape))
        out.append(flat[off:off + size].reshape(shape))
        off += size
    return out


def _split_shards(full, axis):
    shape = full.shape
    s = shape[axis] // N_DEV
    a = full.reshape(shape[:axis] + (N_DEV, s) + shape[axis + 1:])
    return jnp.moveaxis(a, axis, 0).reshape(N_DEV, -1)


def _merge_shards(pieces, shape, axis):
    sh = _shard_shape(shape, axis)
    a = pieces.reshape((N_DEV,) + sh)
    a = jnp.moveaxis(a, 0, axis)
    return a.reshape(shape)


_SCAN_NB = SSM_CH // SCAN_CB


def _to_scan_layout(m, axis):
    shape = m.shape
    m = m.reshape(shape[:axis] + (2, _SCAN_NB, SCAN_CB) + shape[axis + 1:])
    return jnp.swapaxes(m, axis, axis + 1).reshape(shape)


def _from_scan_layout(m, axis):
    shape = m.shape
    m = m.reshape(shape[:axis] + (_SCAN_NB, 2, SCAN_CB) + shape[axis + 1:])
    return jnp.swapaxes(m, axis, axis + 1).reshape(shape)


def _s5_discretise(lam_re, lam_im, b_re, b_im, log_dt):
    dt = jnp.exp(log_dt)[:, None]
    mag = jnp.exp(lam_re * dt)
    ang = lam_im * dt
    lb_re, lb_im = mag * jnp.cos(ang), mag * jnp.sin(ang)
    den = lam_re * lam_re + lam_im * lam_im
    nr, ni = lb_re - 1.0, lb_im
    coef_re = (nr * lam_re + ni * lam_im) / den
    coef_im = (ni * lam_re - nr * lam_im) / den
    bb_re = coef_re[..., None] * b_re - coef_im[..., None] * b_im
    bb_im = coef_re[..., None] * b_im + coef_im[..., None] * b_re
    return lb_re, lb_im, bb_re, bb_im


_GROUPS_PER_BLOCK = N_GROUPS // _SCAN_NB
_U_BLOCK = _GROUPS_PER_BLOCK * SSM_GROUP


def _s5_matrices(lb_re, lb_im, bb_re, bb_im, c_re, c_im):
    eye = jnp.eye(_GROUPS_PER_BLOCK, dtype=F32)
    blocked = lambda m: m.reshape((_SCAN_NB, _GROUPS_PER_BLOCK) + m.shape[1:])
    bmat = lambda bb: jnp.einsum('jgph,gk->jghkp', blocked(bb), eye).reshape(_SCAN_NB, _U_BLOCK, SCAN_CB)
    cmat = lambda cc: jnp.einsum('jghp,gk->jkpgh', blocked(cc), eye).reshape(_SCAN_NB, SCAN_CB, _U_BLOCK)
    b_in = jnp.concatenate([bmat(bb_re), bmat(bb_im)], axis=2)
    c_out = jnp.concatenate([cmat(c_re), -cmat(c_im)], axis=1)
    a_row = _to_scan_layout(jnp.concatenate([lb_re.reshape(1, SSM_CH), lb_im.reshape(1, SSM_CH)], axis=1), 1)
    return b_in, c_out, a_row


def _s5_matrix_grads(db_in, dc_out, da_row):
    da_nat = _from_scan_layout(da_row, 1)
    eye = jnp.eye(_GROUPS_PER_BLOCK, dtype=F32)
    nb, gb = _SCAN_NB, _GROUPS_PER_BLOCK
    bgrad = lambda m: jnp.einsum('jghkp,gk->jgph', m.reshape(nb, gb, SSM_GROUP, gb, SSM_STATE), eye
                                 ).reshape(N_GROUPS, SSM_STATE, SSM_GROUP)
    cgrad = lambda m: jnp.einsum('jkpgh,gk->jghp', m.reshape(nb, gb, SSM_STATE, gb, SSM_GROUP), eye
                                 ).reshape(N_GROUPS, SSM_GROUP, SSM_STATE)
    dbb_re, dbb_im = bgrad(db_in[:, :, :SCAN_CB]), bgrad(db_in[:, :, SCAN_CB:])
    dc_re, dc_im = cgrad(dc_out[:, :SCAN_CB]), -cgrad(dc_out[:, SCAN_CB:])
    dlb_re = da_nat[0, :SSM_CH].reshape(N_GROUPS, SSM_STATE)
    dlb_im = da_nat[0, SSM_CH:].reshape(N_GROUPS, SSM_STATE)
    return dlb_re, dlb_im, dbb_re, dbb_im, dc_re, dc_im


def _as_pieces(a):
    return a.reshape(N_DEV, a.shape[0] // N_DEV, a.shape[1])


def _hybrid_fwd(xn, x, wts, p, weights, riders):
    sv = {}
    hq = _mm(xn, wts['w_qkv_t'], "nt", "l0_in_qkv")
    gate = _mm(xn, wts['w_gate_t'], "nt", "l0_in_gate")
    ba = _mm(xn, wts['w_ba_t'], "nt", "l0_in_ba")
    u = _mm(xn, wts['w_u_t'], "nt", "l0_in_u")
    conv = p['conv_qkv']
    q = _qkv_pre_fwd(hq, conv, 0, 4, True, HEAD_A ** -0.5, "l0_q_pre")
    k = _qkv_pre_fwd(hq, conv, 4, 4, True, 1.0, "l0_k_pre")
    v = _qkv_pre_fwd(hq, conv, 8, 4, False, 1.0, "l0_v_pre")
    gates = _gates_fwd(ba, p['arow'], p['brow'], "l0_gates")
    o, tm_all, s_all = riders.run("l0_gdr_fwd", _gdr_fwd, q, k, v, gates)
    wts['w_glu'], wts['w_out'] = weights.full['w_glu'], weights.full['w_out']
    y_a = _onorm_fwd(o, gate, p['onorm_g'], "l0_onorm")
    bu = riders.run("l0_s5_bu", _mm_bd, u, p['b_in'], "nn")
    xs = riders.run("l0_s5_scan", _s5_scan_fwd, bu, p['a_row'])
    yc = riders.run("l0_s5_cx", _mm_bd, xs, p['c_out'], "nn")
    yl, y_b = _glu_fwd(yc, u, p['d_row'], wts['w_glu'], p['b_glu'], "l0_glu")
    mixed = jnp.concatenate([y_a, y_b], axis=1)
    x1 = _mm(mixed, wts['w_out'], "nn", "l0_out", res=x)
    sv.update(hq=hq, gate=gate, ba=ba, u=u, q=q, k=k, v=v, gb=gates, o=o, tm=tm_all, s=s_all, xs=xs, yl=yl, mixed=mixed)
    return x1, sv


def _hybrid_bwd(dx1, xn, wts, p, sv, riders):
    gr = {}
    dmixed = _mm(dx1, wts['w_out'], "nt", "l0_out_dx", out_dtype=BF16)
    riders.grad('w_out', _as_pieces(_mm(sv['mixed'], dx1, "tn", "l0_out_dw", out_dtype=BF16)))
    dya, dyb = dmixed[:, :WIDTH_A], dmixed[:, WIDTH_A:]
    dyl, du_direct, dw_glu, gr['b_glu_b'], dd = _glu_bwd(
        sv['yl'], sv['u'], p['d_row'], wts['w_glu'], p['b_glu'], dyb, "l0_glu_bwd")
    riders.grad('w_glu', dw_glu.astype(BF16).reshape(N_DEV, -1, PACK_COLS))
    dxs = riders.run("l0_s5_cx_dx", _mm_bd, dyl, p['c_out'], "nt")
    dc_out = _mm_bd(sv['xs'], dyl, "tn", "l0_s5_cx_dw")
    lam, da_row = riders.run("l0_s5_scan_bwd", _s5_scan_bwd, dxs, sv['xs'], p['a_row'])
    du = _mm_bd(lam, p['b_in'], "nt", "l0_s5_bu_dx", res=du_direct, out_dtype=BF16)
    db_in = _mm_bd(sv['u'], lam, "tn", "l0_s5_bu_dw")
    gr['s5'] = (db_in, dc_out, da_row, dd)
    do, dgate, gr['onorm_g_a'] = _onorm_bwd(sv['o'], sv['gate'], p['onorm_g'], dya, "l0_onorm_bwd")
    dq, dk, dv, dgb = riders.run("l0_gdr_bwd", _gdr_bwd, sv['q'], sv['k'], sv['v'], sv['gb'], sv['tm'], sv['s'], do)
    conv = p['conv_qkv']
    dhq_q, dcw_q = _qkv_pre_bwd(sv['hq'], conv, dq, 0, 4, True, HEAD_A ** -0.5, "l0_q_pre_bwd")
    dhq_k, dcw_k = _qkv_pre_bwd(sv['hq'], conv, dk, 4, 4, True, 1.0, "l0_k_pre_bwd")
    dhq_v, dcw_v = _qkv_pre_bwd(sv['hq'], conv, dv, 8, 4, False, 1.0, "l0_v_pre_bwd")
    gr['conv_qkv_a'] = jnp.concatenate([dcw_q, dcw_k, dcw_v], axis=1)
    dhq = jnp.concatenate([dhq_q, dhq_k, dhq_v], axis=1)
    dba, da_log, ddt_bias = _gates_bwd(sv['ba'], p['arow'], p['brow'], dgb, "l0_gates_bwd")
    gr['a_log_a'], gr['dt_bias_a'] = da_log[:, 4:8], ddt_bias[:, 4:8]
    dw_qkv_t = _mm(dhq, xn, "tn", "l0_in_qkv_dw", out_dtype=BF16)
    dw_gate_t = _mm(dgate, xn, "tn", "l0_in_gate_dw", out_dtype=BF16)
    dw_ba_t = _mm(dba, xn, "tn", "l0_in_ba_dw", out_dtype=BF16)
    dw_u_t = _mm(du, xn, "tn", "l0_in_u_dw", out_dtype=BF16)
    dw_in_t = _as_pieces(jnp.concatenate([dw_qkv_t, dw_gate_t, dw_ba_t[:8], dw_u_t], axis=0))
    riders.grad('w_in_t', jnp.concatenate(
        [dw_in_t, jnp.zeros((N_DEV, dict(PIECES)['w_in_t'] - W_IN_PIECE, D_MODEL), BF16)], axis=1))
    dxn = riders.run("l0_in_qkv_dx", _mm, dhq, wts['w_qkv_t'], "nn")
    dxn = _mm(dgate, wts['w_gate_t'], "nn", "l0_in_gate_dx", res=dxn)
    dxn = _mm(dba, wts['w_ba_t'], "nn", "l0_in_ba_dx", res=dxn)
    dxn = _mm(du, wts['w_u_t'], "nn", "l0_in_u_dx", res=dxn)
    return dxn, gr


def _xa_fwd(x1, g, mem_n, wq, wkv_t, wo, tag, riders):
    xq = _rms_fwd(x1, g, BF16, tag + "_norm")
    q = _mm(xq, wq, "nn", tag + "_q", out_dtype=BF16)
    kv = _mm(mem_n, wkv_t, "nt", tag + "_kv", out_dtype=BF16)
    o = riders.run(tag + "_attn", _attn_fwd, q, kv)
    x2 = _mm(o, wo, "nn", tag + "_o", res=x1)
    return x2, dict(xq=xq, q=q, kv=kv, o=o)


def _xa_bwd(dx2, x1, g, mem_n, wq, wkv_t, wo, sv, tag, layer, riders):
    do = _mm(dx2, wo, "nt", tag + "_o_dx", out_dtype=BF16)
    riders.grad('wo%d' % layer, _as_pieces(_mm(sv['o'], dx2, "tn", tag + "_o_dw", out_dtype=BF16)))
    dq, dk, dv = _attn_bwd(sv['q'], sv['kv'], do, tag + "_attn_bwd")
    dkv = jnp.concatenate([dk, dv], axis=1).astype(BF16)
    dxq = _mm(dq, wq, "nt", tag + "_q_dx")
    riders.grad('wq%d' % layer, _as_pieces(_mm(sv['xq'], dq, "tn", tag + "_q_dw", out_dtype=BF16)))
    dmem_n = _mm(dkv, wkv_t, "nn", tag + "_kv_dx")
    riders.grad('wkv_t%d' % layer, _as_pieces(_mm(dkv, mem_n, "tn", tag + "_kv_dw", out_dtype=BF16)))
    dx1, dg = riders.run(tag + "_norm_bwd", _rms_bwd, x1, g, dxq, dx2)
    return dx1, dmem_n, dg


def _ffn_fwd(x2, g, w_up_t, conv, w_down, tag, riders):
    xf = _rms_fwd(x2, g, BF16, tag + "_norm")
    h = riders.run(tag + "_up", _mm, xf, w_up_t, "nt")
    a = riders.run(tag + "_act", _ffn_act_fwd, h, conv)
    x3 = _mm(a, w_down, "nn", tag + "_down", res=x2)
    return x3, dict(xf=xf, h=h, a=a)


def _ffn_bwd(dx3, x2, g, w_up_t, conv, w_down, sv, tag, layer, riders):
    da = _mm(dx3, w_down, "nt", tag + "_down_dx")
    riders.grad('down%d' % layer, _as_pieces(_mm(sv['a'], dx3, "tn", tag + "_down_dw", out_dtype=BF16)))
    dh, dconv = riders.run(tag + "_act_bwd", _ffn_act_bwd, sv['h'], conv, da)
    dxf = riders.run(tag + "_up_dx", _mm, dh, w_up_t, "nn")
    dw_up_t = riders.run(tag + "_up_dw", _mm, dh, sv['xf'], "tn", out_dtype=BF16)
    riders.grad('up_t%d' % layer, _as_pieces(dw_up_t))
    dx2, dg = riders.run(tag + "_norm_bwd", _rms_bwd, x2, g, dxf, dx3)
    return dx2, dconv, dg


BIG_NAMES, SMALL_NAMES, REP_NAMES = list(BIG_SHARDED), list(SMALL_SHARDED), list(REPLICATED)
BIG_SIZES = [int(np.prod(_shard_shape(*BIG_SHARDED[n]))) for n in BIG_NAMES]
SMALL_SIZES = [int(np.prod(_shard_shape(*SMALL_SHARDED[n]))) for n in SMALL_NAMES]


PIECES = [('w_in_t', 384), ('w_glu', 32), ('w_out', 128), ('pool_w', 32), ('wq0', 128), ('wq1', 128),
          ('wkv_t0', 256), ('wkv_t1', 256), ('wo0', 128), ('wo1', 128), ('up_t0', 704), ('up_t1', 704),
          ('down0', 352), ('down1', 352)]
PIECE_OFFS = dict(zip([k for k, _ in PIECES], np.concatenate([[0], np.cumsum([r for _, r in PIECES])[:-1]]).tolist()))
W_IN_ROWS = 4 * WIDTH_A + 2 * N_HEADS_A + SSM_WIDTH
W_IN_PIECE = W_IN_ROWS // N_DEV


def _row_tile(rows):
    return max(t for t in range(16, min(rows, 512) + 1, 16) if rows % t == 0)


class _Riders:
    def __init__(self):
        self.waiting = {}
        self.grads = {}
        self.groups = []
        self.reduced = {}

    def add(self, host, comm, then):
        self.waiting.setdefault(host, []).append((comm, then))

    def run(self, name, fn, *args, **kw):
        riders = self.waiting.pop(name, [])
        if not riders:
            return fn(*args, name=name, **kw)
        out, couts = fn(*args, name=name, comm=[c for c, _ in riders], **kw)
        for (_, then), got in zip(riders, couts):
            then(got)
        return out

    def exchange(self, comm, host, name, then):
        if host is None:
            then(_comm_only(comm, name))
        else:
            self.add(host, comm, then)

    def grad(self, key, pieces):
        self.grads[key] = pieces
        for group in [g for g in self.groups if all(k in self.grads for k in g[1])]:
            self.groups.remove(group)
            self._reduce(*group)

    def _reduce(self, name, keys, swap_host, chips_host):
        arrays = [self.grads[k] for k in keys]
        rows = sum(a.shape[1] for a in arrays)
        tile = _row_tile(rows)

        def after_chips(chip_sums, got):
            total = _chip_sum(chip_sums, got[0], name + "_chip_sum", tr=tile)
            off = 0
            for k, a in zip(keys, arrays):
                self.reduced[k] = total[off:off + a.shape[1]]
                off += a.shape[1]

        def after_swap(got):
            if len(arrays) == 1:
                chip_sums = _pair_sum_pieces(arrays[0], got[0], name + "_pair_sum", tr=tile)
            else:
                core = lax.axis_index("c")
                keep = jnp.concatenate(
                    [lax.dynamic_index_in_dim(a.reshape(4, 2, a.shape[1], PACK_COLS), core, 1, keepdims=False)
                     for a in arrays], axis=1)
                chip_sums = _pair_sum(keep, got[0], name + "_pair_sum", tr=tile)
            self.exchange(_chips_comm(chip_sums), chips_host, name + "_to_chips",
                          functools.partial(after_chips, chip_sums))

        self.exchange(_swap_comm(arrays), swap_host, name + "_to_sibling", after_swap)


class _Weights:
    def __init__(self, inp):
        bf = lambda a: a.astype(BF16)
        local = {'w_in_t': bf(inp['w_in_ab'][0]).T, 'w_glu': bf(inp['w_glu_b'][0]), 'w_out': bf(inp['w_out_ab'][0]),
                 'pool_w': bf(inp['pool_w'][0]),
                 'small': _pack([inp[n] for n in SMALL_NAMES])}
        for l in range(2):
            local['wq%d' % l] = bf(inp['xa_wq'][l])
            local['wkv_t%d' % l] = bf(inp['xa_wkv'][l]).T
            local['wo%d' % l] = bf(inp['xa_wo'][l])
            local['up_t%d' % l] = bf(inp['ffn_w_up'][l]).T
            local['down%d' % l] = bf(inp['ffn_w_down'][l])
        self.local, self.full = local, {}

    def plan(self, keys):
        return _gather_comm([self.local[k] for k in keys])

    def land(self, keys, gathered):
        for k, g in zip(keys, gathered):
            if k == 'small':
                off = 0
                for n, size in zip(SMALL_NAMES, SMALL_SIZES):
                    self.full[n] = _merge_shards(g.reshape(N_DEV, -1)[:, off:off + size], *SMALL_SHARDED[n])
                    off += size
            elif k == 'pool_w':
                self.full[k] = jnp.swapaxes(g, 0, 1).reshape(len(POOL_WINDOWS), POOL_GROUP, POOL_GROUP)
            else:
                self.full[k] = g.reshape(N_DEV * g.shape[1], g.shape[2])


GATHER_FIRST = ['w_in_t', 'small']
GATHER_RIDES = [('l0_gdr_fwd', ['w_glu', 'w_out', 'wq0', 'wkv_t0', 'wo0', 'down0']),
                ('l0_s5_bu', ['pool_w', 'wq1']), ('l0_s5_scan', ['up_t0']), ('l0_s5_cx', ['wo1']),
                ('l0_xa_attn', ['wkv_t1']), ('l0_ffn_up', ['up_t1']), ('l0_ffn_act', ['down1'])]
GRAD_RIDES = [('g_down1', ['down1'], 'l1_ffn_act_bwd', 'l1_ffn_up_dx'),
              ('g_up1', ['up_t1'], 'l1_ffn_norm_bwd', 'l0_ffn_act_bwd'),
              ('g_xa1', ['wq1', 'wkv_t1', 'wo1', 'pool_w'], 'l1_mix_norm_bwd', 'l0_ffn_up_dx'),
              ('g_down0', ['down0'], 'l0_ffn_act_bwd', 'l0_ffn_up_dw'),
              ('g_l0', ['up_t0', 'wq0', 'wkv_t0', 'wo0'], 'l0_xa_norm_bwd', 'l0_gdr_bwd'),
              ('g_out', ['w_out', 'w_glu'], 'l0_s5_cx_dx', 'l0_s5_scan_bwd'),
              ('g_in', ['w_in_t'], 'l0_in_qkv_dx', 'adamw_ffn_w_down')]


def _local_step(inp):
    f32_of = lambda n: inp[n].astype(F32)
    weights = _Weights(inp)
    riders = _Riders()
    riders.groups = list(GRAD_RIDES)
    full = weights.full
    weights.land(GATHER_FIRST, _comm_only(weights.plan(GATHER_FIRST), "gather_first"))
    for host, keys in GATHER_RIDES:
        riders.add(host, weights.plan(keys), functools.partial(weights.land, keys))
    w_in_t = full['w_in_t']
    wts0 = dict(w_qkv_t=w_in_t[:3 * WIDTH_A], w_gate_t=w_in_t[3 * WIDTH_A:4 * WIDTH_A],
                w_ba_t=jnp.concatenate([w_in_t[4 * WIDTH_A:4 * WIDTH_A + 8], jnp.zeros((LANE - 8, D_MODEL), BF16)], 0),
                w_u_t=w_in_t[4 * WIDTH_A + 8:])
    lb_disc, disc_vjp = jax.vjp(_s5_discretise, f32_of('ssm_lambda_re')[0], f32_of('ssm_lambda_im')[0],
                                f32_of('ssm_b_re')[0], f32_of('ssm_b_im')[0], f32_of('ssm_log_dt')[0])
    b_in, c_out, a_row = _s5_matrices(*lb_disc, f32_of('ssm_c_re')[0], f32_of('ssm_c_im')[0])
    zeros4 = jnp.zeros((1, 4), F32)
    p0 = dict(conv_qkv=full['conv_qkv_a'][0], onorm_g=f32_of('onorm_g_a'),
              arow=jnp.concatenate([zeros4, f32_of('a_log_a'), jnp.zeros((1, LANE - 8), F32)], 1),
              brow=jnp.concatenate([zeros4, f32_of('dt_bias_a'), jnp.zeros((1, LANE - 8), F32)], 1),
              b_in=b_in.astype(BF16), c_out=c_out.astype(BF16), a_row=a_row,
              d_row=f32_of('ssm_d').reshape(1, SSM_WIDTH), b_glu=f32_of('b_glu_b'))

    x0 = inp['x'][0]
    mem_n = _rms_fwd(inp['mem'][0], inp['norm_mem_g'], BF16, "mem_norm")
    xn0 = _rms_fwd(x0, inp['norm_mix_g'][0], BF16, "l0_mix_norm")
    x1, sv_mix0 = _hybrid_fwd(xn0, x0, wts0, p0, weights, riders)
    x2, sv_xa0 = _xa_fwd(x1, inp['norm_xa_g'][0], mem_n, full['wq0'], full['wkv_t0'], full['wo0'], "l0_xa", riders)
    x3, sv_ffn0 = _ffn_fwd(x2, inp['norm_ffn_g'][0], full['up_t0'], full['ffn_conv'][0], full['down0'], "l0_ffn", riders)
    xn1 = _rms_fwd(x3, inp['norm_mix_g'][1], F32, "l1_mix_norm")
    x4 = _pool_fwd(xn1, full['pool_w'], full['pool_scale'], x3, "l1_pool")
    x5, sv_xa1 = _xa_fwd(x4, inp['norm_xa_g'][1], mem_n, full['wq1'], full['wkv_t1'], full['wo1'], "l1_xa", riders)
    x6, sv_ffn1 = _ffn_fwd(x5, inp['norm_ffn_g'][1], full['up_t1'], full['ffn_conv'][1], full['down1'], "l1_ffn", riders)
    loss_part, dx6, dg_final = _loss_head(x6, inp['norm_final_g'], inp['loss_target'][0], "loss_head")

    dx5, dconv1, dg_ffn1 = _ffn_bwd(dx6, x5, inp['norm_ffn_g'][1], full['up_t1'], full['ffn_conv'][1], full['down1'],
                                    sv_ffn1, "l1_ffn", 1, riders)
    dx4, dmem1, dg_xa1 = _xa_bwd(dx5, x4, inp['norm_xa_g'][1], mem_n, full['wq1'], full['wkv_t1'], full['wo1'],
                                 sv_xa1, "l1_xa", 1, riders)
    dxn1, dpool_w, dpool_scale = _pool_bwd(xn1, full['pool_w'], full['pool_scale'], dx4, "l1_pool_bwd")
    pool_pieces = jnp.swapaxes(dpool_w.astype(BF16).reshape(len(POOL_WINDOWS), N_DEV, -1, POOL_GROUP), 0, 1)
    riders.grad('pool_w', pool_pieces.reshape(N_DEV, -1, PACK_COLS))
    dx3, dg_mix1 = riders.run("l1_mix_norm_bwd", _rms_bwd, x3, inp['norm_mix_g'][1], dxn1, dx4)
    dx2, dconv0, dg_ffn0 = _ffn_bwd(dx3, x2, inp['norm_ffn_g'][0], full['up_t0'], full['ffn_conv'][0], full['down0'],
                                    sv_ffn0, "l0_ffn", 0, riders)
    dx1, dmem0, dg_xa0 = _xa_bwd(dx2, x1, inp['norm_xa_g'][0], mem_n, full['wq0'], full['wkv_t0'], full['wo0'],
                                 sv_xa0, "l0_xa", 0, riders)
    dxn0, g_mix0 = _hybrid_bwd(dx1, xn0, wts0, p0, sv_mix0, riders)
    grad_x, dg_mix0 = _rms_bwd(x0, inp['norm_mix_g'][0], dxn0, dx1, "l0_mix_norm_bwd")
    _, dg_mem = _rms_bwd(inp['mem'][0], inp['norm_mem_g'], dmem0 + dmem1, None, "mem_norm_bwd")
    assert not riders.groups and all(k.startswith("adamw_") for k in riders.waiting), (list(riders.waiting), riders.groups)

    db_in, dc_out, da_row, dd = g_mix0['s5']
    dlb_re, dlb_im, dbb_re, dbb_im, dc_re, dc_im = _s5_matrix_grads(db_in, dc_out, da_row)
    dlam_re, dlam_im, dbr, dbi, dlog_dt = disc_vjp((dlb_re, dlb_im, dbb_re, dbb_im))

    rep_grads = {
        'norm_mix_g': jnp.concatenate([dg_mix0, dg_mix1], 0), 'norm_xa_g': jnp.concatenate([dg_xa0, dg_xa1], 0),
        'norm_ffn_g': jnp.concatenate([dg_ffn0, dg_ffn1], 0), 'norm_mem_g': dg_mem.reshape(-1),
        'norm_final_g': dg_final.reshape(-1), 'a_log_a': g_mix0['a_log_a'], 'dt_bias_a': g_mix0['dt_bias_a'],
        'onorm_g_a': g_mix0['onorm_g_a'], 'ssm_lambda_re': dlam_re[None], 'ssm_lambda_im': dlam_im[None],
        'ssm_b_re': dbr[None], 'ssm_b_im': dbi[None], 'ssm_c_re': dc_re[None], 'ssm_c_im': dc_im[None],
        'ssm_d': dd.reshape(1, N_GROUPS, SSM_GROUP), 'ssm_log_dt': dlog_dt[None], 'b_glu_b': g_mix0['b_glu_b']}
    small_grads = {'conv_qkv_a': g_mix0['conv_qkv_a'][None], 'pool_scale': dpool_scale,
                   'ffn_conv': jnp.stack([dconv0, dconv1])}
    return loss_part, grad_x, riders, rep_grads, small_grads


ADAMW_ORDER = ['ffn_w_up', 'ffn_w_down', 'xa_wkv', 'xa_wq', 'xa_wo', 'w_out_ab', 'w_glu_b', 'pool_w', 'w_in_ab']
SMALL_GRADS_RIDE_ON = "adamw_ffn_w_up"


def _update(inp, loss_part, grad_x, riders, rep_grads, small_grads):
    dev = _device_index()
    gathered = {}
    misc_local = _pack([rep_grads[n] for n in REP_NAMES] + [small_grads[n] for n in SMALL_NAMES] + [loss_part])
    riders.add(SMALL_GRADS_RIDE_ON, _gather_comm([misc_local]), lambda got: gathered.update(misc=got[0]))
    piece = lambda key: riders.reduced[key]
    both = lambda name: jnp.stack([piece(name + '0'), piece(name + '1')])
    swap = lambda a: jnp.swapaxes(a, -1, -2)
    reduced = {'w_in_ab': lambda: piece('w_in_t')[:W_IN_PIECE][None],
               'w_glu_b': lambda: piece('w_glu').reshape(inp['w_glu_b'].shape),
               'w_out_ab': lambda: piece('w_out')[None], 'pool_w': lambda: piece('pool_w').reshape(inp['pool_w'].shape),
               'xa_wq': lambda: both('wq'), 'xa_wkv': lambda: both('wkv_t'), 'xa_wo': lambda: both('wo'),
               'ffn_w_up': lambda: both('up_t'), 'ffn_w_down': lambda: both('down')}
    transposed = ('w_in_ab', 'xa_wkv', 'ffn_w_up')
    grads, upd = {}, {}
    assert sorted(ADAMW_ORDER) == sorted(BIG_NAMES)
    for n in ADAMW_ORDER:
        fix = swap if n in transposed else (lambda a: a)
        g = reduced[n]()
        out = riders.run("adamw_" + n, _adamw, fix(inp[n]), g, fix(inp['m_' + n]), fix(inp['v_' + n]))
        upd[n], grads[n] = tuple(fix(o) for o in out), fix(g)
    assert not riders.waiting, list(riders.waiting)
    misc_sum = _sum_leading(gathered['misc'], "small_grads_sum")
    misc = _unpack(misc_sum, [inp[n].shape for n in REP_NAMES] + [SMALL_SHARDED[n][0] for n in SMALL_NAMES] + [()])
    loss = misc.pop()
    for n, g in zip(REP_NAMES, misc):
        grads[n] = g
    for n, g in zip(SMALL_NAMES, misc[len(REP_NAMES):]):
        grads[n] = lax.dynamic_index_in_dim(_split_shards(g, SMALL_SHARDED[n][1]), dev, 0, keepdims=False
                                            ).reshape(inp[n].shape)
    tiny_names = REP_NAMES + SMALL_NAMES
    rep_total = sum(int(np.prod(inp[n].shape)) for n in REP_NAMES)
    packs = [_pack([inp[prefix + n] for n in tiny_names]) for prefix in ('', 'm_', 'v_')]
    g_pack = _pack([misc_sum.reshape(-1)[:rep_total]] + [grads[n] for n in SMALL_NAMES])
    tiny_out = [_unpack(o, [inp[n].shape for n in tiny_names])
                for o in _adamw(packs[0], g_pack, packs[1], packs[2], "adamw_small")]
    for i, n in enumerate(tiny_names):
        upd[n] = tuple(o[i] for o in tiny_out)

    outs = [loss, grad_x[None]]
    outs += [grads[n] for n in WEIGHT_NAMES]
    for i in range(3):
        outs += [upd[n][i] for n in WEIGHT_NAMES]
    return tuple(outs)


def _step(inp):
    loss_part, grad_x, riders, rep_grads, small_grads = _local_step(inp)
    return _update(inp, loss_part, grad_x, riders, rep_grads, small_grads)


INPUT_NAMES = (['x', 'mem'] + WEIGHT_NAMES + ['loss_target'] + ['m_' + n for n in WEIGHT_NAMES]
               + ['v_' + n for n in WEIGHT_NAMES])


def kernel(x, mem, norm_mix_g, norm_xa_g, norm_ffn_g, norm_mem_g, norm_final_g, w_in_ab, conv_qkv_a, a_log_a, dt_bias_a, onorm_g_a, ssm_lambda_re, ssm_lambda_im, ssm_b_re, ssm_b_im, ssm_c_re, ssm_c_im, ssm_d, ssm_log_dt, w_glu_b, b_glu_b, w_out_ab, pool_w, pool_scale, xa_wq, xa_wkv, xa_wo, ffn_w_up, ffn_conv, ffn_w_down, loss_target, m_norm_mix_g, m_norm_xa_g, m_norm_ffn_g, m_norm_mem_g, m_norm_final_g, m_w_in_ab, m_conv_qkv_a, m_a_log_a, m_dt_bias_a, m_onorm_g_a, m_ssm_lambda_re, m_ssm_lambda_im, m_ssm_b_re, m_ssm_b_im, m_ssm_c_re, m_ssm_c_im, m_ssm_d, m_ssm_log_dt, m_w_glu_b, m_b_glu_b, m_w_out_ab, m_pool_w, m_pool_scale, m_xa_wq, m_xa_wkv, m_xa_wo, m_ffn_w_up, m_ffn_conv, m_ffn_w_down, v_norm_mix_g, v_norm_xa_g, v_norm_ffn_g, v_norm_mem_g, v_norm_final_g, v_w_in_ab, v_conv_qkv_a, v_a_log_a, v_dt_bias_a, v_onorm_g_a, v_ssm_lambda_re, v_ssm_lambda_im, v_ssm_b_re, v_ssm_b_im, v_ssm_c_re, v_ssm_c_im, v_ssm_d, v_ssm_log_dt, v_w_glu_b, v_b_glu_b, v_w_out_ab, v_pool_w, v_pool_scale, v_xa_wq, v_xa_wkv, v_xa_wo, v_ffn_w_up, v_ffn_conv, v_ffn_w_down):
    args = (x, mem, norm_mix_g, norm_xa_g, norm_ffn_g, norm_mem_g, norm_final_g, w_in_ab, conv_qkv_a, a_log_a, dt_bias_a, onorm_g_a, ssm_lambda_re, ssm_lambda_im, ssm_b_re, ssm_b_im, ssm_c_re, ssm_c_im, ssm_d, ssm_log_dt, w_glu_b, b_glu_b, w_out_ab, pool_w, pool_scale, xa_wq, xa_wkv, xa_wo, ffn_w_up, ffn_conv, ffn_w_down, loss_target, m_norm_mix_g, m_norm_xa_g, m_norm_ffn_g, m_norm_mem_g, m_norm_final_g, m_w_in_ab, m_conv_qkv_a, m_a_log_a, m_dt_bias_a, m_onorm_g_a, m_ssm_lambda_re, m_ssm_lambda_im, m_ssm_b_re, m_ssm_b_im, m_ssm_c_re, m_ssm_c_im, m_ssm_d, m_ssm_log_dt, m_w_glu_b, m_b_glu_b, m_w_out_ab, m_pool_w, m_pool_scale, m_xa_wq, m_xa_wkv, m_xa_wo, m_ffn_w_up, m_ffn_conv, m_ffn_w_down, v_norm_mix_g, v_norm_xa_g, v_norm_ffn_g, v_norm_mem_g, v_norm_final_g, v_w_in_ab, v_conv_qkv_a, v_a_log_a, v_dt_bias_a, v_onorm_g_a, v_ssm_lambda_re, v_ssm_lambda_im, v_ssm_b_re, v_ssm_b_im, v_ssm_c_re, v_ssm_c_im, v_ssm_d, v_ssm_log_dt, v_w_glu_b, v_b_glu_b, v_w_out_ab, v_pool_w, v_pool_scale, v_xa_wq, v_xa_wkv, v_xa_wo, v_ffn_w_up, v_ffn_conv, v_ffn_w_down)
    return _step(dict(zip(INPUT_NAMES, args)))
```

```python
import functools
import math

import numpy as np
import jax
import jax.numpy as jnp
from jax import lax
from jax.experimental import pallas as pl
from jax.experimental.pallas import tpu as pltpu
from jax.experimental.pallas import tpu_sc as plsc

F32, BF16 = jnp.float32, jnp.bfloat16
HIGH, HIGHEST = lax.Precision.HIGH, lax.Precision.HIGHEST
MESH = pl.DeviceIdType.MESH

N_DEV = 8
SEQ, D_MODEL, MEM_LEN = 2048, 1024, 256
WIDTH_A, N_HEADS_A, HEAD_A, CONV_A = 512, 4, 128, 4
GDR_CHUNK = 128
GDR_HEADS_PER_STEP = 4
SSM_WIDTH, SSM_GROUP, N_GROUPS, SSM_STATE = 512, 16, 32, 64
SSM_CH = N_GROUPS * SSM_STATE
SCAN_CB = 512
POOL_WINDOWS = (2, 4, 8, 16)
POOL_GROUP = 256
N_HEADS_X, HEAD_X = 4, 256
D_FF, CONV_FFN = 2816, 3
RMS_EPS = 1e-6
ADAM_LR, ADAM_B1, ADAM_B2, ADAM_EPS, ADAM_WD, ADAM_STEP = 0.001, 0.9, 0.999, 1e-08, 0.01, 10
LANE = 128
PACK_COLS = 1024
VMEM_LIMIT_BYTES = 56 * 1024 * 1024


def _params(sem=None):
    return pltpu.CompilerParams(dimension_semantics=sem, vmem_limit_bytes=VMEM_LIMIT_BYTES)


class Comm:
    def __init__(self, inputs, out_shapes, sems, start, end, mid=None):
        self.inputs, self.out_shapes, self.sems = list(inputs), list(out_shapes), list(sems)
        self.start, self.mid, self.end = start, mid, end


def _merge_comms(comms):
    comms = [c for c in comms if c is not None]
    if not comms:
        return None, []
    bounds, ni, no, ns = [], 0, 0, 0
    for c in comms:
        bounds.append((ni, no, ns))
        ni, no, ns = ni + len(c.inputs), no + len(c.out_shapes), ns + len(c.sems)

    def phase(which):
        def run(ins, outs, sems):
            for c, (i0, o0, s0) in zip(comms, bounds):
                fn = getattr(c, which)
                if fn is not None:
                    fn(ins[i0:i0 + len(c.inputs)], outs[o0:o0 + len(c.out_shapes)], sems[s0:s0 + len(c.sems)])
        return run

    merged = Comm([a for c in comms for a in c.inputs], [s for c in comms for s in c.out_shapes],
                  [s for c in comms for s in c.sems], phase("start"), phase("end"), phase("mid"))
    return merged, [(o0, o0 + len(c.out_shapes)) for c, (_, o0, _) in zip(comms, bounds)]


def _call(body, *, name, grid, in_specs, out_specs, out_shape, args, scratch_shapes=(), sem=None, comm=None):
    single = not isinstance(out_shape, (list, tuple))
    out_specs_l = [out_specs] if single else list(out_specs)
    out_shape_l = [out_shape] if single else list(out_shape)
    scratch_shapes = list(scratch_shapes)
    merged, spans = _merge_comms(comm if isinstance(comm, (list, tuple)) else [comm])
    if merged is None:
        outs = pl.pallas_call(body, name=name, grid=grid, in_specs=list(in_specs), out_specs=out_specs_l,
                              out_shape=out_shape_l, scratch_shapes=scratch_shapes, compiler_params=_params(sem))(*args)
        outs = outs[0] if single else outs
        return outs if comm is None else (outs, [])
    n_in, n_out, n_scr = len(in_specs), len(out_specs_l), len(scratch_shapes)
    ci, co = len(merged.inputs), len(merged.out_shapes)
    total = int(np.prod(grid))

    def wrapped(*refs):
        ins, cins = refs[:n_in], refs[n_in:n_in + ci]
        outs, couts = refs[n_in + ci:n_in + ci + n_out], refs[n_in + ci + n_out:n_in + ci + n_out + co]
        scr, csems = refs[n_in + ci + n_out + co:n_in + ci + n_out + co + n_scr], refs[n_in + ci + n_out + co + n_scr:]
        lin = pl.program_id(0)
        for d in range(1, len(grid)):
            lin = lin * grid[d] + pl.program_id(d)
        pl.when(lin == 0)(lambda: merged.start(cins, couts, csems))
        body(*ins, *outs, *scr)
        mid_step = min((3 * total) // 4, total - 1)
        pl.when(lin == mid_step)(lambda: merged.mid(cins, couts, csems))
        pl.when(lin == total - 1)(lambda: merged.end(cins, couts, csems))

    any_spec = pl.BlockSpec(memory_space=pl.ANY)
    res = pl.pallas_call(
        wrapped, name=name, grid=grid, in_specs=list(in_specs) + [any_spec] * ci,
        out_specs=out_specs_l + [any_spec] * co, out_shape=out_shape_l + merged.out_shapes,
        scratch_shapes=scratch_shapes + merged.sems,
        compiler_params=_params(("arbitrary",) * len(grid)))(*args, *merged.inputs)
    outs, couts = res[:n_out], res[n_out:]
    return (outs[0] if single else list(outs)), [list(couts[a:b]) for a, b in spans]


def _comm_only(comm, name):
    def body():
        pass

    _, couts = _call(body, name=name, grid=(1,), in_specs=[], out_specs=[], out_shape=[], args=[], comm=comm)
    return couts[0]


def _tile(dim, pref):
    best = None
    for t in range(LANE, min(dim, pref) + 1, LANE):
        if dim % t == 0:
            best = t
    return best if best is not None else dim


MM_VMEM_BUDGET = 40 * 1024 * 1024


def _mm_tiles(m, n, k, a_bytes, b_bytes, o_bytes, r_bytes):
    for tk in (k, _tile(k, 2048), _tile(k, 1024), _tile(k, 512)):
        for tm, tn in ((1024, 1536), (1024, 1024), (1024, 512), (512, 512), (256, 512), (256, 256)):
            tm, tn = _tile(m, tm), _tile(n, tn)
            acc = 0 if tk == k else tm * tn * 4
            need = 2 * (tm * tk * a_bytes + tk * tn * b_bytes + tm * tn * (o_bytes + r_bytes)) + acc
            if need <= MM_VMEM_BUDGET:
                return tm, tn, tk
    raise ValueError("no matmul tiling fits VMEM")


def _mm(a, b, mode, name, out_dtype=F32, res=None, comm=None):
    if mode == "nn":
        (m, k), n = a.shape, b.shape[1]
    elif mode == "nt":
        (m, k), n = a.shape, b.shape[0]
    else:
        (k, m), n = a.shape, b.shape[1]
    tm, tn, tk = _mm_tiles(m, n, k, a.dtype.itemsize, b.dtype.itemsize, jnp.dtype(out_dtype).itemsize,
                           0 if res is None else res.dtype.itemsize)
    nk = k // tk
    dims = {"nn": ((1,), (0,)), "nt": ((1,), (1,)), "tn": ((0,), (0,))}[mode]

    def body(*refs):
        if res is None:
            a_ref, b_ref, o_ref = refs[:3]
            r_ref = None
        else:
            a_ref, b_ref, r_ref, o_ref = refs[:4]
        part = lax.dot_general(a_ref[...].astype(BF16), b_ref[...].astype(BF16), (dims, ((), ())),
                               preferred_element_type=F32)

        def finish(out):
            if r_ref is not None:
                out = out + r_ref[...].astype(F32)
            o_ref[...] = out.astype(out_dtype)

        if nk == 1:
            finish(part)
            return
        acc = refs[-1]
        kk = pl.program_id(2)

        @pl.when(kk == 0)
        def _():
            acc[...] = part

        @pl.when(kk > 0)
        def _():
            acc[...] += part

        @pl.when(kk == nk - 1)
        def _():
            finish(acc[...])

    a_spec = (pl.BlockSpec((tk, tm), lambda i, j, q: (q, i)) if mode == "tn"
              else pl.BlockSpec((tm, tk), lambda i, j, q: (i, q)))
    b_spec = (pl.BlockSpec((tn, tk), lambda i, j, q: (j, q)) if mode == "nt"
              else pl.BlockSpec((tk, tn), lambda i, j, q: (q, j)))
    o_spec = pl.BlockSpec((tm, tn), lambda i, j, q: (i, j))
    in_specs, args = [a_spec, b_spec], [a, b]
    if res is not None:
        in_specs.append(o_spec)
        args.append(res)
    return _call(body, name=name, grid=(m // tm, n // tn, nk), in_specs=in_specs, out_specs=o_spec,
                 out_shape=jax.ShapeDtypeStruct((m, n), out_dtype),
                 scratch_shapes=[] if nk == 1 else [pltpu.VMEM((tm, tn), F32)],
                 sem=("parallel", "parallel", "arbitrary"), args=args, comm=comm)


def _mm_bd(a, b, mode, name, out_dtype=F32, res=None, comm=None, tm=1024):
    if mode == "tn":
        k = a.shape[0]
        nb = min(a.shape[1], b.shape[1]) // LANE
        ma, n = a.shape[1] // nb, b.shape[1] // nb

        def body(a_ref, b_ref, o_ref):
            o_ref[0] = lax.dot_general(a_ref[...].astype(BF16), b_ref[...].astype(BF16), (((0,), (0,)), ((), ())),
                                       preferred_element_type=F32).astype(out_dtype)

        return _call(body, name=name, grid=(nb,),
                     in_specs=[pl.BlockSpec((k, ma), lambda j: (0, j)), pl.BlockSpec((k, n), lambda j: (0, j))],
                     out_specs=pl.BlockSpec((1, ma, n), lambda j: (j, 0, 0)),
                     out_shape=jax.ShapeDtypeStruct((nb, ma, n), out_dtype), sem=("parallel",), args=(a, b), comm=comm)
    m = a.shape[0]
    nb = b.shape[0]
    ka = a.shape[1] // nb
    n = b.shape[2] if mode == "nn" else b.shape[1]
    tm = _tile(m, tm)
    dims = ((1,), (0,)) if mode == "nn" else ((1,), (1,))

    def body(*refs):
        if res is None:
            a_ref, b_ref, o_ref = refs
            r_ref = None
        else:
            a_ref, b_ref, r_ref, o_ref = refs
        out = lax.dot_general(a_ref[...].astype(BF16), b_ref[0].astype(BF16), (dims, ((), ())),
                              preferred_element_type=F32)
        if r_ref is not None:
            out = out + r_ref[...].astype(F32)
        o_ref[...] = out.astype(out_dtype)

    o_spec = pl.BlockSpec((tm, n), lambda i, j: (i, j))
    in_specs = [pl.BlockSpec((tm, ka), lambda i, j: (i, j)), pl.BlockSpec((1,) + b.shape[1:], lambda i, j: (j, 0, 0))]
    args = [a, b]
    if res is not None:
        in_specs.append(o_spec)
        args.append(res)
    return _call(body, name=name, grid=(m // tm, nb), in_specs=in_specs, out_specs=o_spec,
                 out_shape=jax.ShapeDtypeStruct((m, nb * n), out_dtype), sem=("parallel", "parallel"),
                 args=args, comm=comm)


def _rms_fwd(x, g, out_dtype, name, tr=256):
    rows, d = x.shape

    def body(x_ref, g_ref, o_ref):
        xv = x_ref[...]
        r = lax.rsqrt(jnp.mean(xv * xv, axis=-1, keepdims=True) + RMS_EPS)
        o_ref[...] = (xv * r * g_ref[...]).astype(out_dtype)

    return pl.pallas_call(
        body, name=name, grid=(rows // tr,),
        in_specs=[pl.BlockSpec((tr, d), lambda i: (i, 0)), pl.BlockSpec((1, d), lambda i: (0, 0))],
        out_specs=pl.BlockSpec((tr, d), lambda i: (i, 0)), out_shape=jax.ShapeDtypeStruct((rows, d), out_dtype),
        compiler_params=_params(("parallel",)))(x, g.reshape(1, d))


def _rms_bwd(x, g, dy, dres, name, tr=256, comm=None):
    rows, d = x.shape

    def body(*refs):
        if dres is None:
            x_ref, g_ref, dy_ref, dx_ref, dg_ref = refs
            r_ref = None
        else:
            x_ref, g_ref, dy_ref, r_ref, dx_ref, dg_ref = refs

        @pl.when(pl.program_id(0) == 0)
        def _():
            dg_ref[...] = jnp.zeros_like(dg_ref)

        xv, dyv = x_ref[...], dy_ref[...].astype(F32)
        r = lax.rsqrt(jnp.mean(xv * xv, axis=-1, keepdims=True) + RMS_EPS)
        xh = xv * r
        dyg = dyv * g_ref[...]
        dx = r * (dyg - xh * jnp.mean(dyg * xh, axis=-1, keepdims=True))
        if r_ref is not None:
            dx = dx + r_ref[...]
        dx_ref[...] = dx
        dg_ref[...] += jnp.sum(dyv * xh, axis=0, keepdims=True)

    blk = pl.BlockSpec((tr, d), lambda i: (i, 0))
    vec = pl.BlockSpec((1, d), lambda i: (0, 0))
    in_specs, args = [blk, vec, blk], [x, g.reshape(1, d), dy]
    if dres is not None:
        in_specs.append(blk)
        args.append(dres)
    return _call(
        body, name=name, grid=(rows // tr,), in_specs=in_specs, out_specs=[blk, vec],
        out_shape=[jax.ShapeDtypeStruct((rows, d), F32), jax.ShapeDtypeStruct((1, d), F32)],
        sem=("arbitrary",), args=args, comm=comm)


def _loss_head(x, g, target, name, tr=256):
    rows, d = x.shape

    def body(x_ref, g_ref, t_ref, loss_ref, dx_ref, dg_ref):
        @pl.when(pl.program_id(0) == 0)
        def _():
            dg_ref[...] = jnp.zeros_like(dg_ref)
            loss_ref[...] = jnp.zeros_like(loss_ref)

        xv = x_ref[...]
        r = lax.rsqrt(jnp.mean(xv * xv, axis=-1, keepdims=True) + RMS_EPS)
        xh = xv * r
        err = xh * g_ref[...] - t_ref[...]
        loss_ref[...] += 0.5 * jnp.sum(jnp.mean(err * err, axis=-1, keepdims=True), keepdims=True)
        dyv = err * (1.0 / d)
        dyg = dyv * g_ref[...]
        dx_ref[...] = r * (dyg - xh * jnp.mean(dyg * xh, axis=-1, keepdims=True))
        dg_ref[...] += jnp.sum(dyv * xh, axis=0, keepdims=True)

    blk = pl.BlockSpec((tr, d), lambda i: (i, 0))
    vec = pl.BlockSpec((1, d), lambda i: (0, 0))
    return pl.pallas_call(
        body, name=name, grid=(rows // tr,), in_specs=[blk, vec, blk],
        out_specs=[pl.BlockSpec((1, 1), lambda i: (0, 0)), blk, vec],
        out_shape=[jax.ShapeDtypeStruct((1, 1), F32), jax.ShapeDtypeStruct((rows, d), F32),
                   jax.ShapeDtypeStruct((1, d), F32)],
        compiler_params=_params(("arbitrary",)))(x, g.reshape(1, d), target)


def _shift_down(x, s):
    rows = lax.broadcasted_iota(jnp.int32, x.shape, 0)
    return jnp.where(rows >= s, pltpu.roll(x, s, 0), 0.0)


def _shift_up(x, s):
    n = x.shape[0]
    rows = lax.broadcasted_iota(jnp.int32, x.shape, 0)
    return jnp.where(rows < n - s, pltpu.roll(x, n - s, 0), 0.0)


def _sigmoid(x):
    return 1.0 / (1.0 + jnp.exp(-x))


def _silu_and_grad(x):
    s = _sigmoid(x)
    return x * s, s * (1.0 + x * (1.0 - s))


_GELU_C0, _GELU_C1 = math.sqrt(2.0 / math.pi), 0.044715


def _gelu_and_grad(x):
    th = jnp.tanh(_GELU_C0 * (x + _GELU_C1 * x * x * x))
    y = 0.5 * x * (1.0 + th)
    dy = 0.5 * (1.0 + th) + 0.5 * x * (1.0 - th * th) * _GELU_C0 * (1.0 + 3.0 * _GELU_C1 * x * x)
    return y, dy


def _ffn_act_fwd(h, w, name, tc=256, comm=None):
    t = h.shape[0]
    nb = D_FF // tc

    def body(hg_ref, hv_ref, wg_ref, wv_ref, a_ref):
        def conv(x, wr):
            return wr[2:3, :] * x + wr[1:2, :] * _shift_down(x, 1) + wr[0:1, :] * _shift_down(x, 2)

        cg = conv(hg_ref[...], wg_ref[...])
        cv = conv(hv_ref[...], wv_ref[...])
        a_ref[...] = (cg * _sigmoid(cg) * cv).astype(BF16)

    return _call(
        body, name=name, grid=(nb,),
        in_specs=[pl.BlockSpec((t, tc), lambda j: (0, j)), pl.BlockSpec((t, tc), lambda j: (0, j + nb)),
                  pl.BlockSpec((CONV_FFN, tc), lambda j: (0, j)), pl.BlockSpec((CONV_FFN, tc), lambda j: (0, j + nb))],
        out_specs=pl.BlockSpec((t, tc), lambda j: (0, j)), out_shape=jax.ShapeDtypeStruct((t, D_FF), BF16),
        sem=("parallel",), args=(h, h, w, w), comm=comm)


def _ffn_act_bwd(h, w, da, name, tc=256, comm=None):
    t = h.shape[0]
    nb = D_FF // tc

    def body(hg_ref, hv_ref, wg_ref, wv_ref, da_ref, dhg_ref, dhv_ref, dwg_ref, dwv_ref):
        hg, hv, wg, wv = hg_ref[...], hv_ref[...], wg_ref[...], wv_ref[...]
        hg1, hg2, hv1, hv2 = _shift_down(hg, 1), _shift_down(hg, 2), _shift_down(hv, 1), _shift_down(hv, 2)
        cg = wg[2:3, :] * hg + wg[1:2, :] * hg1 + wg[0:1, :] * hg2
        cv = wv[2:3, :] * hv + wv[1:2, :] * hv1 + wv[0:1, :] * hv2
        sg, dsg = _silu_and_grad(cg)
        dav = da_ref[...].astype(F32)
        dcv = dav * sg
        dcg = dav * cv * dsg

        def conv_t(dc, wr):
            return wr[2:3, :] * dc + wr[1:2, :] * _shift_up(dc, 1) + wr[0:1, :] * _shift_up(dc, 2)

        dhg_ref[...] = conv_t(dcg, wg).astype(BF16)
        dhv_ref[...] = conv_t(dcv, wv).astype(BF16)
        dwg_ref[0:1, :] = jnp.sum(dcg * hg2, axis=0, keepdims=True)
        dwg_ref[1:2, :] = jnp.sum(dcg * hg1, axis=0, keepdims=True)
        dwg_ref[2:3, :] = jnp.sum(dcg * hg, axis=0, keepdims=True)
        dwv_ref[0:1, :] = jnp.sum(dcv * hv2, axis=0, keepdims=True)
        dwv_ref[1:2, :] = jnp.sum(dcv * hv1, axis=0, keepdims=True)
        dwv_ref[2:3, :] = jnp.sum(dcv * hv, axis=0, keepdims=True)

    big = lambda off: pl.BlockSpec((t, tc), lambda j: (0, j + off))
    small = lambda off: pl.BlockSpec((CONV_FFN, tc), lambda j: (0, j + off))
    res = _call(
        body, name=name, grid=(nb,),
        in_specs=[big(0), big(nb), small(0), small(nb), big(0)],
        out_specs=[big(0), big(0), small(0), small(0)],
        out_shape=[jax.ShapeDtypeStruct((t, D_FF), BF16), jax.ShapeDtypeStruct((t, D_FF), BF16),
                   jax.ShapeDtypeStruct((CONV_FFN, D_FF), F32), jax.ShapeDtypeStruct((CONV_FFN, D_FF), F32)],
        sem=("parallel",), args=(h, h, w, w, da), comm=comm)
    (dhg, dhv, dwg, dwv), couts = res if comm is not None else (res, None)
    out = (jnp.concatenate([dhg, dhv], axis=1), jnp.concatenate([dwg, dwv], axis=1))
    return out if comm is None else (out, couts)


def _attn_probs(q, k):
    s = lax.dot_general(q.astype(BF16), k.astype(BF16), (((1,), (1,)), ((), ())),
                        preferred_element_type=F32) * (HEAD_X ** -0.5)
    s = s - jnp.max(s, axis=-1, keepdims=True)
    p = jnp.exp(s)
    return p / jnp.sum(p, axis=-1, keepdims=True)


def _attn_fwd(q, kv, name, tq=512, comm=None):
    t = q.shape[0]

    def body(q_ref, k_ref, v_ref, o_ref):
        p = _attn_probs(q_ref[...], k_ref[...])
        o_ref[...] = jnp.dot(p.astype(BF16), v_ref[...].astype(BF16), preferred_element_type=F32).astype(BF16)

    return _call(
        body, name=name, grid=(N_HEADS_X, t // tq),
        in_specs=[pl.BlockSpec((tq, HEAD_X), lambda h, i: (i, h)),
                  pl.BlockSpec((MEM_LEN, HEAD_X), lambda h, i: (0, h)),
                  pl.BlockSpec((MEM_LEN, HEAD_X), lambda h, i: (0, h + N_HEADS_X))],
        out_specs=pl.BlockSpec((tq, HEAD_X), lambda h, i: (i, h)),
        out_shape=jax.ShapeDtypeStruct((t, N_HEADS_X * HEAD_X), BF16),
        sem=("parallel", "parallel"), args=(q, kv, kv), comm=comm)


def _attn_bwd(q, kv, do, name, tq=512):
    t = q.shape[0]

    def body(q_ref, k_ref, v_ref, do_ref, dq_ref, dk_ref, dv_ref):
        @pl.when(pl.program_id(1) == 0)
        def _():
            dk_ref[...] = jnp.zeros_like(dk_ref)
            dv_ref[...] = jnp.zeros_like(dv_ref)

        qb, kb, vb, dob = (r[...].astype(BF16) for r in (q_ref, k_ref, v_ref, do_ref))
        p = _attn_probs(qb, kb)
        dp = lax.dot_general(dob, vb, (((1,), (1,)), ((), ())), preferred_element_type=F32)
        ds = p * (dp - jnp.sum(dp * p, axis=-1, keepdims=True)) * (HEAD_X ** -0.5)
        dsb = ds.astype(BF16)
        dq_ref[...] = jnp.dot(dsb, kb, preferred_element_type=F32).astype(BF16)
        dk_ref[...] += lax.dot_general(dsb, qb, (((0,), (0,)), ((), ())), preferred_element_type=F32)
        dv_ref[...] += lax.dot_general(p.astype(BF16), dob, (((0,), (0,)), ((), ())), preferred_element_type=F32)

    qs = pl.BlockSpec((tq, HEAD_X), lambda h, i: (i, h))
    ms = pl.BlockSpec((MEM_LEN, HEAD_X), lambda h, i: (0, h))
    return pl.pallas_call(
        body, name=name, grid=(N_HEADS_X, t // tq),
        in_specs=[qs, ms, pl.BlockSpec((MEM_LEN, HEAD_X), lambda h, i: (0, h + N_HEADS_X)), qs],
        out_specs=[qs, ms, ms],
        out_shape=[jax.ShapeDtypeStruct((t, D_MODEL), BF16), jax.ShapeDtypeStruct((MEM_LEN, D_MODEL), F32),
                   jax.ShapeDtypeStruct((MEM_LEN, D_MODEL), F32)],
        compiler_params=_params(("parallel", "arbitrary")))(q, kv, kv, do)


def _pool_counts(t, win):
    pos = lax.broadcasted_iota(jnp.int32, (t, 1), 0).astype(F32) + 1.0
    return 1.0 / jnp.minimum(pos, float(win))


def _pool_delta(xv, win):
    s, step = xv, 1
    while step < win:
        s = s + _shift_down(s, step)
        step *= 2
    return s * _pool_counts(xv.shape[0], win) - xv


def _pool_delta_t(dv, win):
    s, step = dv * _pool_counts(dv.shape[0], win), 1
    while step < win:
        s = s + _shift_up(s, step)
        step *= 2
    return s - dv


def _pool_fwd(xn, w, scale, res, name):
    t = xn.shape[0]

    def make_branch(win, xn_ref, w_ref, s_ref, r_ref, o_ref):
        def branch():
            dl = _pool_delta(xn_ref[...], win)
            y = jnp.dot(dl.astype(BF16), w_ref[0], preferred_element_type=F32)
            o_ref[...] = r_ref[...] + y * s_ref[...]
        return branch

    def body(xn_ref, w_ref, s_ref, r_ref, o_ref):
        for gi, win in enumerate(POOL_WINDOWS):
            pl.when(pl.program_id(0) == gi)(make_branch(win, xn_ref, w_ref, s_ref, r_ref, o_ref))

    blk = pl.BlockSpec((t, POOL_GROUP), lambda g: (0, g))
    return pl.pallas_call(
        body, name=name, grid=(len(POOL_WINDOWS),),
        in_specs=[blk, pl.BlockSpec((1, POOL_GROUP, POOL_GROUP), lambda g: (g, 0, 0)),
                  pl.BlockSpec((1, POOL_GROUP), lambda g: (0, g)), blk],
        out_specs=blk, out_shape=jax.ShapeDtypeStruct((t, D_MODEL), F32),
        compiler_params=_params(("parallel",)))(xn, w, scale, res)


def _pool_bwd(xn, w, scale, dmix, name):
    t = xn.shape[0]

    def make_branch(win, xn_ref, w_ref, s_ref, d_ref, dxn_ref, dw_ref, ds_ref):
        def branch():
            dl = _pool_delta(xn_ref[...], win).astype(BF16)
            wv = w_ref[0]
            dm = d_ref[...]
            y = jnp.dot(dl, wv, preferred_element_type=F32)
            ds_ref[...] = jnp.sum(dm * y, axis=0, keepdims=True)
            dy = (dm * s_ref[...]).astype(BF16)
            dw_ref[0] = lax.dot_general(dl, dy, (((0,), (0,)), ((), ())), preferred_element_type=F32)
            ddl = lax.dot_general(dy, wv, (((1,), (1,)), ((), ())), preferred_element_type=F32)
            dxn_ref[...] = _pool_delta_t(ddl, win)
        return branch

    def body(*refs):
        for gi, win in enumerate(POOL_WINDOWS):
            pl.when(pl.program_id(0) == gi)(make_branch(win, *refs))

    blk = pl.BlockSpec((t, POOL_GROUP), lambda g: (0, g))
    wspec = pl.BlockSpec((1, POOL_GROUP, POOL_GROUP), lambda g: (g, 0, 0))
    vec = pl.BlockSpec((1, POOL_GROUP), lambda g: (0, g))
    return pl.pallas_call(
        body, name=name, grid=(len(POOL_WINDOWS),), in_specs=[blk, wspec, vec, blk], out_specs=[blk, wspec, vec],
        out_shape=[jax.ShapeDtypeStruct((t, D_MODEL), F32),
                   jax.ShapeDtypeStruct((len(POOL_WINDOWS), POOL_GROUP, POOL_GROUP), F32),
                   jax.ShapeDtypeStruct((1, D_MODEL), F32)],
        compiler_params=_params(("parallel",)))(xn, w, scale, dmix)


def _qkv_conv(h, wr):
    return (wr[3:4, :] * h + wr[2:3, :] * _shift_down(h, 1) + wr[1:2, :] * _shift_down(h, 2)
            + wr[0:1, :] * _shift_down(h, 3))


def _qkv_pre_fwd(h, w, col0, ncols, normalize, scale, name):
    t = h.shape[0]

    def body(h_ref, w_ref, o_ref):
        c = _qkv_conv(h_ref[...], w_ref[...])
        s = c * _sigmoid(c)
        if normalize:
            s = s * lax.rsqrt(jnp.sum(s * s, axis=-1, keepdims=True) + 1e-6) * scale
        o_ref[...] = s

    return pl.pallas_call(
        body, name=name, grid=(ncols,),
        in_specs=[pl.BlockSpec((t, HEAD_A), lambda j: (0, j + col0)), pl.BlockSpec((CONV_A, HEAD_A), lambda j: (0, j + col0))],
        out_specs=pl.BlockSpec((t, HEAD_A), lambda j: (0, j)), out_shape=jax.ShapeDtypeStruct((t, ncols * HEAD_A), F32),
        compiler_params=_params(("parallel",)))(h, w)


def _qkv_pre_bwd(h, w, dy, col0, ncols, normalize, scale, name):
    t = h.shape[0]

    def body(h_ref, w_ref, dy_ref, dh_ref, dw_ref):
        hv, wr, dyv = h_ref[...], w_ref[...], dy_ref[...]
        h1, h2, h3 = _shift_down(hv, 1), _shift_down(hv, 2), _shift_down(hv, 3)
        c = wr[3:4, :] * hv + wr[2:3, :] * h1 + wr[1:2, :] * h2 + wr[0:1, :] * h3
        s, dsilu = _silu_and_grad(c)
        if normalize:
            r = lax.rsqrt(jnp.sum(s * s, axis=-1, keepdims=True) + 1e-6)
            y = s * r
            dyv = dyv * scale
            ds = r * (dyv - y * jnp.sum(dyv * y, axis=-1, keepdims=True))
        else:
            ds = dyv
        dc = ds * dsilu
        dh = (wr[3:4, :] * dc + wr[2:3, :] * _shift_up(dc, 1) + wr[1:2, :] * _shift_up(dc, 2)
              + wr[0:1, :] * _shift_up(dc, 3))
        dh_ref[...] = dh.astype(BF16)
        dw_ref[0:1, :] = jnp.sum(dc * h3, axis=0, keepdims=True)
        dw_ref[1:2, :] = jnp.sum(dc * h2, axis=0, keepdims=True)
        dw_ref[2:3, :] = jnp.sum(dc * h1, axis=0, keepdims=True)
        dw_ref[3:4, :] = jnp.sum(dc * hv, axis=0, keepdims=True)

    return pl.pallas_call(
        body, name=name, grid=(ncols,),
        in_specs=[pl.BlockSpec((t, HEAD_A), lambda j: (0, j + col0)), pl.BlockSpec((CONV_A, HEAD_A), lambda j: (0, j + col0)),
                  pl.BlockSpec((t, HEAD_A), lambda j: (0, j))],
        out_specs=[pl.BlockSpec((t, HEAD_A), lambda j: (0, j)), pl.BlockSpec((CONV_A, HEAD_A), lambda j: (0, j))],
        out_shape=[jax.ShapeDtypeStruct((t, ncols * HEAD_A), BF16), jax.ShapeDtypeStruct((CONV_A, ncols * HEAD_A), F32)],
        compiler_params=_params(("parallel",)))(h, w, dy)


def _softplus(x):
    return jnp.maximum(x, 0.0) + jnp.log1p(jnp.exp(-jnp.abs(x)))


def _gates_fwd(ba, arow, brow, name):
    t = ba.shape[0]

    def body(x_ref, a_ref, b_ref, o_ref):
        xv = x_ref[...]
        lane = lax.broadcasted_iota(jnp.int32, xv.shape, 1)
        beta = _sigmoid(xv)
        g = -jnp.exp(a_ref[...]) * _softplus(xv + b_ref[...])
        o_ref[...] = jnp.where(lane < N_HEADS_A, beta, jnp.where(lane < 2 * N_HEADS_A, g, 0.0))

    return pl.pallas_call(body, name=name, out_shape=jax.ShapeDtypeStruct((t, LANE), F32),
                          compiler_params=_params())(ba, arow, brow)


def _gates_bwd(ba, arow, brow, dgb, name):
    t = ba.shape[0]

    def body(x_ref, a_ref, b_ref, d_ref, dx_ref, da_ref, db_ref):
        xv = x_ref[...]
        dv = d_ref[0] + d_ref[1] + d_ref[2] + d_ref[3]
        lane = lax.broadcasted_iota(jnp.int32, xv.shape, 1)
        beta = _sigmoid(xv)
        ea = jnp.exp(a_ref[...])
        z = xv + b_ref[...]
        dgv = jnp.where((lane >= N_HEADS_A) & (lane < 2 * N_HEADS_A), dv, 0.0) * (-ea)
        dz = dgv * _sigmoid(z)
        dx = jnp.where(lane < N_HEADS_A, dv * beta * (1.0 - beta), dz)
        dx_ref[...] = dx.astype(BF16)
        db_ref[...] = jnp.sum(dz, axis=0, keepdims=True)
        da_ref[...] = jnp.sum(dgv * _softplus(z), axis=0, keepdims=True)

    return pl.pallas_call(
        body, name=name,
        out_shape=[jax.ShapeDtypeStruct((t, LANE), BF16), jax.ShapeDtypeStruct((1, LANE), F32),
                   jax.ShapeDtypeStruct((1, LANE), F32)],
        compiler_params=_params())(ba, arow, brow, dgb)


def _dot(a, b, prec=None):
    if prec is None:
        return jnp.dot(a.astype(BF16), b.astype(BF16), preferred_element_type=F32)
    return jnp.dot(a, b, precision=prec, preferred_element_type=F32)


def _dot_nt(a, b, prec=None):
    if prec is None:
        a, b = a.astype(BF16), b.astype(BF16)
    return lax.dot_general(a, b, (((1,), (1,)), ((), ())), precision=prec, preferred_element_type=F32)


def _dot_tn(a, b, prec=None):
    if prec is None:
        a, b = a.astype(BF16), b.astype(BF16)
    return lax.dot_general(a, b, (((0,), (0,)), ((), ())), precision=prec, preferred_element_type=F32)


def _gdr_chunk_terms(k, beta, g):
    c = GDR_CHUNK
    row = lax.broadcasted_iota(jnp.int32, (c, c), 0)
    col = lax.broadcasted_iota(jnp.int32, (c, c), 1)
    causal, strict = row >= col, row > col
    gcum = _dot(causal.astype(F32), jnp.broadcast_to(g, (c, c)), HIGHEST)
    diff = gcum - gcum.T
    decay = jnp.where(causal, jnp.exp(jnp.where(causal, diff, 0.0)), 0.0)
    kb = k * beta
    kk = _dot_nt(kb, k)
    return row, col, causal, strict, gcum, decay, kb, kk


def _unit_lower_inverse(a):
    c = a.shape[0]
    eye = (lax.broadcasted_iota(jnp.int32, (c, c), 0) == lax.broadcasted_iota(jnp.int32, (c, c), 1)).astype(F32)
    p = -a
    inv = eye + p
    step = 1
    while 2 * step < c:
        p = _dot(p, p, HIGH)
        inv = inv + _dot(inv, p, HIGH)
        step *= 2
    return inv


def _head_gates(gates, head):
    lane = lax.broadcasted_iota(jnp.int32, gates.shape, 1)
    beta = jnp.sum(jnp.where(lane == head, gates, 0.0), axis=1, keepdims=True)
    g = jnp.sum(jnp.where(lane == head + N_HEADS_A, gates, 0.0), axis=1, keepdims=True)
    return beta, g


def _gdr_fwd(q, k, v, gates, name, comm=None):
    t = q.shape[0]
    c = GDR_CHUNK
    n = t // c

    hps = GDR_HEADS_PER_STEP

    def one_head(hh, q_ref, k_ref, v_ref, gb_ref, o_ref, tm_ref, s_ref, state):
        cols = slice(hh * HEAD_A, (hh + 1) * HEAD_A)
        qv, kv, vv = q_ref[:, cols], k_ref[:, cols], v_ref[:, cols]
        beta, g = _head_gates(gb_ref[...], pl.program_id(0) * hps + hh)
        row, col, causal, strict, gcum, decay, kb, kk = _gdr_chunk_terms(kv, beta, g)
        tm = _unit_lower_inverse(jnp.where(strict, kk * decay, 0.0))
        e = jnp.exp(gcum)
        u = _dot(tm, vv * beta, HIGH)
        w = _dot(tm, kb * e, HIGH)
        p = jnp.where(causal, _dot_nt(qv, kv) * decay, 0.0)
        s = state[hh]
        s_ref[hh, 0] = s
        tm_ref[hh, 0] = tm
        vn = u - _dot(w, s)
        o_ref[:, cols] = _dot(qv * e, s) + _dot(p, vn)
        glast = gcum[c - 1:c, :]
        state[hh] = s * jnp.exp(glast) + _dot_tn(kv * jnp.exp(glast - gcum), vn)

    def body(*refs):
        state = refs[-1]

        @pl.when(pl.program_id(1) == 0)
        def _():
            state[...] = jnp.zeros_like(state)

        for hh in range(hps):
            one_head(hh, *refs)

    blk = pl.BlockSpec((c, hps * HEAD_A), lambda h, i: (i, h))
    mat = pl.BlockSpec((hps, 1, c, c), lambda h, i: (h, i, 0, 0))
    return _call(
        body, name=name, grid=(N_HEADS_A // hps, n),
        in_specs=[blk, blk, blk, pl.BlockSpec((c, LANE), lambda h, i: (i, 0))],
        out_specs=[blk, mat, mat],
        out_shape=[jax.ShapeDtypeStruct((t, WIDTH_A), F32), jax.ShapeDtypeStruct((N_HEADS_A, n, c, c), F32),
                   jax.ShapeDtypeStruct((N_HEADS_A, n, HEAD_A, HEAD_A), F32)],
        scratch_shapes=[pltpu.VMEM((hps, HEAD_A, HEAD_A), F32)], sem=("parallel", "arbitrary"),
        args=(q, k, v, gates), comm=comm)


def _gdr_bwd(q, k, v, gates, tm_all, s_all, do, name, comm=None):
    t = q.shape[0]
    c = GDR_CHUNK
    n = t // c

    hps = GDR_HEADS_PER_STEP

    def one_head(hh, q_ref, k_ref, v_ref, gb_ref, tm_ref, s_ref, do_ref, dq_ref, dk_ref, dv_ref, dgb_ref, dstate):
        cols = slice(hh * HEAD_A, (hh + 1) * HEAD_A)
        qv, kv, vv, dov = q_ref[:, cols], k_ref[:, cols], v_ref[:, cols], do_ref[:, cols]
        head = pl.program_id(0) * hps + hh
        beta, g = _head_gates(gb_ref[...], head)
        tm, s, dsp = tm_ref[hh, 0], s_ref[hh, 0], dstate[hh]
        row, col, causal, strict, gcum, decay, kb, kk = _gdr_chunk_terms(kv, beta, g)
        e = jnp.exp(gcum)
        vb, kbe = vv * beta, kb * e
        u = _dot(tm, vb, HIGH)
        w = _dot(tm, kbe, HIGH)
        qk = _dot_nt(qv, kv)
        p = jnp.where(causal, qk * decay, 0.0)
        vn = u - _dot(w, s)
        glast = gcum[c - 1:c, :]
        el = jnp.exp(glast)
        f = jnp.exp(glast - gcum)
        kd = kv * f
        qe = qv * e

        dvn = _dot_tn(p, dov) + _dot(kd, dsp)
        dglast = el[:, 0:1] * jnp.sum(s * dsp, keepdims=True)
        dkd = _dot_nt(vn, dsp)
        dk = dkd * f
        df = jnp.sum(dkd * kv, axis=1, keepdims=True) * f[:, 0:1]
        dglast = dglast + jnp.sum(df, keepdims=True)
        dgc = -df
        dp = jnp.where(causal, _dot_nt(dov, vn), 0.0)
        dqe = _dot_nt(dov, s)
        dq = dqe * e
        de = jnp.sum(dqe * qv, axis=1, keepdims=True)
        dstate[hh] = dsp * el + _dot_tn(qe, dov) - _dot_tn(w, dvn)
        dw = -_dot_nt(dvn, s)
        dvb = _dot_tn(tm, dvn, HIGH)
        dkbe = _dot_tn(tm, dw, HIGH)
        da = -jnp.where(strict, _dot_nt(dvb, u) + _dot_nt(dkbe, w), 0.0)
        dkk = da * decay
        dqk = dp * decay
        dd = da * kk + dp * qk
        dq = dq + _dot(dqk, kv)
        dk = dk + _dot_tn(dqk, qv)
        dkb = _dot(dkk, kv) + dkbe * e
        dk = dk + _dot_tn(dkk, kb)
        de = de + jnp.sum(dkbe * kb, axis=1, keepdims=True)
        dk = dk + dkb * beta
        dbeta = jnp.sum(dkb * kv, axis=1, keepdims=True) + jnp.sum(dvb * vv, axis=1, keepdims=True)
        m = dd * decay
        dgc = dgc + jnp.sum(m, axis=1, keepdims=True) - jnp.sum(m.T, axis=1, keepdims=True)
        dgc = dgc + de * e[:, 0:1]
        dgc = dgc + jnp.where(row[:, 0:1] == c - 1, dglast, 0.0)
        dg = _dot((row <= col).astype(F32), jnp.broadcast_to(dgc, (c, c)), HIGHEST)
        dq_ref[:, cols] = dq
        dk_ref[:, cols] = dk
        dv_ref[:, cols] = dvb * beta
        lane = lax.broadcasted_iota(jnp.int32, (c, LANE), 1)
        dgb_ref[hh] = jnp.where(lane == head, dbeta, jnp.where(lane == head + N_HEADS_A, dg, 0.0))

    def body(*refs):
        dstate = refs[-1]

        @pl.when(pl.program_id(1) == 0)
        def _():
            dstate[...] = jnp.zeros_like(dstate)

        for hh in range(hps):
            one_head(hh, *refs)

    blk = pl.BlockSpec((c, hps * HEAD_A), lambda h, i: (n - 1 - i, h))
    mat = pl.BlockSpec((hps, 1, c, c), lambda h, i: (h, n - 1 - i, 0, 0))
    return _call(
        body, name=name, grid=(N_HEADS_A // hps, n),
        in_specs=[blk, blk, blk, pl.BlockSpec((c, LANE), lambda h, i: (n - 1 - i, 0)), mat, mat, blk],
        out_specs=[blk, blk, blk, pl.BlockSpec((hps, c, LANE), lambda h, i: (h, n - 1 - i, 0))],
        out_shape=[jax.ShapeDtypeStruct((t, WIDTH_A), F32)] * 3 + [jax.ShapeDtypeStruct((N_HEADS_A, t, LANE), F32)],
        scratch_shapes=[pltpu.VMEM((hps, HEAD_A, HEAD_A), F32)], sem=("parallel", "arbitrary"),
        args=(q, k, v, gates, tm_all, s_all, do), comm=comm)


_B_NN, _B_NT, _B_TN = ((2,), (1,)), ((2,), (2,)), ((1,), (1,))


def _bdot(a, b, dims=_B_NN, prec=None):
    if prec is None:
        a, b = a.astype(BF16), b.astype(BF16)
    return lax.dot_general(a, b, (dims, ((0,), (0,))), precision=prec, preferred_element_type=F32)


def _heads_of(ref):
    return jnp.stack([ref[:, h * HEAD_A:(h + 1) * HEAD_A] for h in range(N_HEADS_A)])


def _all_head_gates(gates):
    pairs = [_head_gates(gates, h) for h in range(N_HEADS_A)]
    return jnp.stack([b for b, _ in pairs]), jnp.stack([g for _, g in pairs])


def _gdr_terms(k, beta, g):
    h, c = k.shape[0], GDR_CHUNK
    row = lax.broadcasted_iota(jnp.int32, (c, c), 0)
    col = lax.broadcasted_iota(jnp.int32, (c, c), 1)
    causal, strict = row >= col, row > col
    lower = jnp.broadcast_to(causal.astype(F32), (h, c, c))
    gcum = _bdot(lower, jnp.broadcast_to(g, (h, c, c)), prec=HIGHEST)
    diff = gcum - jnp.swapaxes(gcum, 1, 2)
    decay = jnp.where(causal, jnp.exp(jnp.where(causal, diff, 0.0)), 0.0)
    kb = k * beta
    return row, col, causal, strict, gcum, decay, kb, _bdot(kb, k, _B_NT)


def _unit_lower_inverses(a):
    c = a.shape[1]
    eye = (lax.broadcasted_iota(jnp.int32, (c, c), 0) == lax.broadcasted_iota(jnp.int32, (c, c), 1)).astype(F32)
    p = -a
    inv = eye + p
    step = 1
    while 2 * step < c:
        p = _bdot(p, p, prec=HIGH)
        inv = inv + _bdot(inv, p, prec=HIGH)
        step *= 2
    return inv


def _gdr_fwd(q, k, v, gates, name, comm=None):
    t = q.shape[0]
    c, nh = GDR_CHUNK, N_HEADS_A
    n = t // c

    def body(q_ref, k_ref, v_ref, gb_ref, o_ref, tm_ref, s_ref, state):
        @pl.when(pl.program_id(0) == 0)
        def _():
            state[...] = jnp.zeros_like(state)

        qv, kv, vv = _heads_of(q_ref), _heads_of(k_ref), _heads_of(v_ref)
        beta, g = _all_head_gates(gb_ref[...])
        row, col, causal, strict, gcum, decay, kb, kk = _gdr_terms(kv, beta, g)
        tm = _unit_lower_inverses(jnp.where(strict, kk * decay, 0.0))
        e = jnp.exp(gcum)
        u = _bdot(tm, vv * beta, prec=HIGH)
        w = _bdot(tm, kb * e, prec=HIGH)
        p = jnp.where(causal, _bdot(qv, kv, _B_NT) * decay, 0.0)
        s = state[...]
        s_ref[:, 0] = s
        tm_ref[:, 0] = tm
        vn = u - _bdot(w, s)
        o = _bdot(qv * e, s) + _bdot(p, vn)
        for h in range(nh):
            o_ref[:, h * HEAD_A:(h + 1) * HEAD_A] = o[h]
        glast = gcum[:, c - 1:c, :]
        state[...] = s * jnp.exp(glast) + _bdot(kv * jnp.exp(glast - gcum), vn, _B_TN)

    blk = pl.BlockSpec((c, WIDTH_A), lambda i: (i, 0))
    mat = pl.BlockSpec((nh, 1, c, c), lambda i: (0, i, 0, 0))
    return _call(
        body, name=name, grid=(n,), in_specs=[blk, blk, blk, pl.BlockSpec((c, LANE), lambda i: (i, 0))],
        out_specs=[blk, mat, mat],
        out_shape=[jax.ShapeDtypeStruct((t, WIDTH_A), F32), jax.ShapeDtypeStruct((nh, n, c, c), F32),
                   jax.ShapeDtypeStruct((nh, n, HEAD_A, HEAD_A), F32)],
        scratch_shapes=[pltpu.VMEM((nh, HEAD_A, HEAD_A), F32)], sem=("arbitrary",),
        args=(q, k, v, gates), comm=comm)


def _gdr_bwd(q, k, v, gates, tm_all, s_all, do, name, comm=None):
    t = q.shape[0]
    c, nh = GDR_CHUNK, N_HEADS_A
    n = t // c

    def body(q_ref, k_ref, v_ref, gb_ref, tm_ref, s_ref, do_ref, dq_ref, dk_ref, dv_ref, dgb_ref, dstate):
        @pl.when(pl.program_id(0) == 0)
        def _():
            dstate[...] = jnp.zeros_like(dstate)

        qv, kv, vv, dov = _heads_of(q_ref), _heads_of(k_ref), _heads_of(v_ref), _heads_of(do_ref)
        beta, g = _all_head_gates(gb_ref[...])
        tm, s, dsp = tm_ref[:, 0], s_ref[:, 0], dstate[...]
        row, col, causal, strict, gcum, decay, kb, kk = _gdr_terms(kv, beta, g)
        rowsum = lambda x: jnp.sum(x, axis=2, keepdims=True)
        e = jnp.exp(gcum)
        vb, kbe = vv * beta, kb * e
        u = _bdot(tm, vb, prec=HIGH)
        w = _bdot(tm, kbe, prec=HIGH)
        qk = _bdot(qv, kv, _B_NT)
        p = jnp.where(causal, qk * decay, 0.0)
        vn = u - _bdot(w, s)
        glast = gcum[:, c - 1:c, :]
        el = jnp.exp(glast)
        f = jnp.exp(glast - gcum)
        kd = kv * f
        qe = qv * e

        dvn = _bdot(p, dov, _B_TN) + _bdot(kd, dsp)
        dglast = el[:, :, 0:1] * jnp.sum(s * dsp, axis=(1, 2), keepdims=True)
        dkd = _bdot(vn, dsp, _B_NT)
        dk = dkd * f
        df = rowsum(dkd * kv) * f[:, :, 0:1]
        dglast = dglast + jnp.sum(df, axis=1, keepdims=True)
        dgc = -df
        dp = jnp.where(causal, _bdot(dov, vn, _B_NT), 0.0)
        dqe = _bdot(dov, s, _B_NT)
        dq = dqe * e
        de = rowsum(dqe * qv)
        dstate[...] = dsp * el + _bdot(qe, dov, _B_TN) - _bdot(w, dvn, _B_TN)
        dw = -_bdot(dvn, s, _B_NT)
        dvb = _bdot(tm, dvn, _B_TN, prec=HIGH)
        dkbe = _bdot(tm, dw, _B_TN, prec=HIGH)
        da = -jnp.where(strict, _bdot(dvb, u, _B_NT) + _bdot(dkbe, w, _B_NT), 0.0)
        dkk = da * decay
        dqk = dp * decay
        dd = da * kk + dp * qk
        dq = dq + _bdot(dqk, kv)
        dk = dk + _bdot(dqk, qv, _B_TN)
        dkb = _bdot(dkk, kv) + dkbe * e
        dk = dk + _bdot(dkk, kb, _B_TN)
        de = de + rowsum(dkbe * kb)
        dk = dk + dkb * beta
        dbeta = rowsum(dkb * kv) + rowsum(dvb * vv)
        m = dd * decay
        dgc = dgc + rowsum(m) - rowsum(jnp.swapaxes(m, 1, 2))
        dgc = dgc + de * e[:, :, 0:1]
        dgc = dgc + jnp.where(row[:, 0:1] == c - 1, dglast, 0.0)
        upper = jnp.broadcast_to((row <= col).astype(F32), (nh, c, c))
        dg = _bdot(upper, jnp.broadcast_to(dgc, (nh, c, c)), prec=HIGHEST)
        dv = dvb * beta
        for h in range(nh):
            cols = slice(h * HEAD_A, (h + 1) * HEAD_A)
            dq_ref[:, cols] = dq[h]
            dk_ref[:, cols] = dk[h]
            dv_ref[:, cols] = dv[h]
        head = lax.broadcasted_iota(jnp.int32, (nh, c, LANE), 0)
        lane = lax.broadcasted_iota(jnp.int32, (nh, c, LANE), 2)
        dgb_ref[...] = jnp.where(lane == head, dbeta, jnp.where(lane == head + nh, dg, 0.0))

    blk = pl.BlockSpec((c, WIDTH_A), lambda i: (n - 1 - i, 0))
    mat = pl.BlockSpec((nh, 1, c, c), lambda i: (0, n - 1 - i, 0, 0))
    return _call(
        body, name=name, grid=(n,),
        in_specs=[blk, blk, blk, pl.BlockSpec((c, LANE), lambda i: (n - 1 - i, 0)), mat, mat, blk],
        out_specs=[blk, blk, blk, pl.BlockSpec((nh, c, LANE), lambda i: (0, n - 1 - i, 0))],
        out_shape=[jax.ShapeDtypeStruct((t, WIDTH_A), F32)] * 3 + [jax.ShapeDtypeStruct((nh, t, LANE), F32)],
        scratch_shapes=[pltpu.VMEM((nh, HEAD_A, HEAD_A), F32)], sem=("arbitrary",),
        args=(q, k, v, gates, tm_all, s_all, do), comm=comm)


def _onorm_fwd(o, gate, g, name):
    t = o.shape[0]

    def body(o_ref, gate_ref, g_ref, y_ref):
        ov, gv = o_ref[...], gate_ref[...]
        r = lax.rsqrt(jnp.mean(ov * ov, axis=-1, keepdims=True) + RMS_EPS)
        y_ref[...] = (ov * r * g_ref[...] * gv * _sigmoid(gv)).astype(BF16)

    blk = pl.BlockSpec((t, HEAD_A), lambda j: (0, j))
    return pl.pallas_call(
        body, name=name, grid=(N_HEADS_A,), in_specs=[blk, blk, pl.BlockSpec((1, HEAD_A), lambda j: (0, 0))],
        out_specs=blk, out_shape=jax.ShapeDtypeStruct((t, WIDTH_A), BF16),
        compiler_params=_params(("parallel",)))(o, gate, g)


def _onorm_bwd(o, gate, g, dy, name):
    t = o.shape[0]

    def body(o_ref, gate_ref, g_ref, dy_ref, do_ref, dgate_ref, dg_ref):
        @pl.when(pl.program_id(0) == 0)
        def _():
            dg_ref[...] = jnp.zeros_like(dg_ref)

        ov, gv, dyv = o_ref[...], gate_ref[...], dy_ref[...].astype(F32)
        r = lax.rsqrt(jnp.mean(ov * ov, axis=-1, keepdims=True) + RMS_EPS)
        oh = ov * r
        sg, dsg = _silu_and_grad(gv)
        dgate_ref[...] = (dyv * oh * g_ref[...] * dsg).astype(BF16)
        dn = dyv * sg
        dg_ref[...] += jnp.sum(dn * oh, axis=0, keepdims=True)
        dng = dn * g_ref[...]
        do_ref[...] = r * (dng - oh * jnp.mean(dng * oh, axis=-1, keepdims=True))

    blk = pl.BlockSpec((t, HEAD_A), lambda j: (0, j))
    vec = pl.BlockSpec((1, HEAD_A), lambda j: (0, 0))
    return pl.pallas_call(
        body, name=name, grid=(N_HEADS_A,), in_specs=[blk, blk, vec, blk], out_specs=[blk, blk, vec],
        out_shape=[jax.ShapeDtypeStruct((t, WIDTH_A), F32), jax.ShapeDtypeStruct((t, WIDTH_A), BF16),
                   jax.ShapeDtypeStruct((1, HEAD_A), F32)],
        compiler_params=_params(("arbitrary",)))(o, gate, g, dy)


def _cmul(ar, ai, br, bi):
    return ar * br - ai * bi, ar * bi + ai * br


def _scan_tables(ar, ai, reverse):
    p1 = (ar, ai)
    p2 = _cmul(*p1, *p1)
    p4 = _cmul(*p2, *p2)
    p8 = _cmul(*p4, *p4)
    p3 = _cmul(*p2, *p1)
    p5 = _cmul(*p4, *p1)
    p6 = _cmul(*p4, *p2)
    p7 = _cmul(*p4, *p3)
    pows = [p1, p2, p3, p4, p5, p6, p7, p8]
    rows = lax.broadcasted_iota(jnp.int32, (8, ar.shape[1]), 0)
    tr = jnp.zeros((8, ar.shape[1]), F32)
    ti = jnp.zeros((8, ar.shape[1]), F32)
    for r in range(8):
        pw = pows[7 - r] if reverse else pows[r]
        tr = jnp.where(rows == r, pw[0], tr)
        ti = jnp.where(rows == r, pw[1], ti)
    return p1, p2, p4, p8, tr, ti


def _tile_scan(xr, xi, p1, p2, p4, reverse):
    rows = lax.broadcasted_iota(jnp.int32, xr.shape, 0)
    for s, (pr, pi) in ((1, p1), (2, p2), (4, p4)):
        if reverse:
            keep = rows < 8 - s
            sr, si = pltpu.roll(xr, 8 - s, 0), pltpu.roll(xi, 8 - s, 0)
        else:
            keep = rows >= s
            sr, si = pltpu.roll(xr, s, 0), pltpu.roll(xi, s, 0)
        sr, si = jnp.where(keep, sr, 0.0), jnp.where(keep, si, 0.0)
        mr, mi = _cmul(pr, pi, sr, si)
        xr, xi = xr + mr, xi + mi
    return xr, xi


def _s5_scan_fwd(bu, a, name, tb=512, comm=None):
    t = bu.shape[0]
    cb = SCAN_CB
    nt = t // tb

    def body(b_ref, a_ref, x_ref, carry):
        @pl.when(pl.program_id(1) == 0)
        def _():
            carry[...] = jnp.zeros_like(carry)

        ar, ai = a_ref[:, 0:cb], a_ref[:, cb:2 * cb]
        p1, p2, p4, p8, tr, ti = _scan_tables(ar, ai, False)

        def step(j, c):
            cr, ci = c
            i = pl.multiple_of(j * 8, 8)
            xr, xi = _tile_scan(b_ref[pl.ds(i, 8), 0:cb], b_ref[pl.ds(i, 8), cb:2 * cb], p1, p2, p4, False)
            mr, mi = _cmul(tr, ti, cr, ci)
            xr, xi = xr + mr, xi + mi
            x_ref[pl.ds(i, 8), 0:cb] = xr
            x_ref[pl.ds(i, 8), cb:2 * cb] = xi
            return xr[7:8, :], xi[7:8, :]

        cr, ci = lax.fori_loop(0, tb // 8, step, (carry[0:1, :], carry[1:2, :]), unroll=2)
        carry[0:1, :] = cr
        carry[1:2, :] = ci

    blk = pl.BlockSpec((tb, 2 * cb), lambda j, i: (i, j))
    return _call(
        body, name=name, grid=(SSM_CH // cb, nt),
        in_specs=[blk, pl.BlockSpec((1, 2 * cb), lambda j, i: (0, j))], out_specs=blk,
        out_shape=jax.ShapeDtypeStruct((t, 2 * SSM_CH), F32), scratch_shapes=[pltpu.VMEM((8, cb), F32)],
        sem=("parallel", "arbitrary"), args=(bu, a), comm=comm)


def _s5_scan_bwd(dx, x, a, name, tb=512, comm=None):
    t = dx.shape[0]
    cb = SCAN_CB
    nt = t // tb
    nj = tb // 8

    def body(d_ref, x_ref, xp_ref, a_ref, l_ref, da_ref, carry, acc):
        tblk = pl.program_id(1)

        @pl.when(tblk == 0)
        def _():
            carry[...] = jnp.zeros_like(carry)
            acc[...] = jnp.zeros_like(acc)

        ar, ai = a_ref[:, 0:cb], a_ref[:, cb:2 * cb]
        p1, p2, p4, p8, tr, ti = _scan_tables(ar, -ai, True)
        rows = lax.broadcasted_iota(jnp.int32, (8, cb), 0)

        def step(jj, c):
            cr, ci, sr_acc, si_acc = c
            j = nj - 1 - jj
            i = pl.multiple_of(j * 8, 8)
            lr, li = _tile_scan(d_ref[pl.ds(i, 8), 0:cb], d_ref[pl.ds(i, 8), cb:2 * cb], p1, p2, p4, True)
            mr, mi = _cmul(tr, ti, cr, ci)
            lr, li = lr + mr, li + mi
            l_ref[pl.ds(i, 8), 0:cb] = lr
            l_ref[pl.ds(i, 8), cb:2 * cb] = li
            ip = pl.multiple_of(jnp.maximum(j - 1, 0) * 8, 8)
            prev_r = jnp.where(j > 0, x_ref[pl.ds(ip, 8), 0:cb], xp_ref[:, 0:cb])
            prev_i = jnp.where(j > 0, x_ref[pl.ds(ip, 8), cb:2 * cb], xp_ref[:, cb:2 * cb])
            edge = jnp.where(jnp.logical_and(j == 0, tblk == nt - 1), 0.0, 1.0)
            xs_r = jnp.where(rows == 0, pltpu.roll(prev_r, 1, 0) * edge, pltpu.roll(x_ref[pl.ds(i, 8), 0:cb], 1, 0))
            xs_i = jnp.where(rows == 0, pltpu.roll(prev_i, 1, 0) * edge, pltpu.roll(x_ref[pl.ds(i, 8), cb:2 * cb], 1, 0))
            sr_acc = sr_acc + lr * xs_r + li * xs_i
            si_acc = si_acc + li * xs_r - lr * xs_i
            return lr[0:1, :], li[0:1, :], sr_acc, si_acc

        cr, ci, sr_acc, si_acc = lax.fori_loop(
            0, nj, step, (carry[0:1, :], carry[1:2, :], acc[:, 0:cb], acc[:, cb:2 * cb]))
        carry[0:1, :] = cr
        carry[1:2, :] = ci
        acc[:, 0:cb] = sr_acc
        acc[:, cb:2 * cb] = si_acc

        @pl.when(tblk == nt - 1)
        def _():
            da_ref[...] = jnp.sum(acc[...], axis=0, keepdims=True)

    blk = pl.BlockSpec((tb, 2 * cb), lambda j, i: (nt - 1 - i, j))
    prev = pl.BlockSpec((8, 2 * cb), lambda j, i: (jnp.maximum((nt - 1 - i) * (tb // 8) - 1, 0), j))
    vec = pl.BlockSpec((1, 2 * cb), lambda j, i: (0, j))
    return _call(
        body, name=name, grid=(SSM_CH // cb, nt), in_specs=[blk, blk, prev, vec], out_specs=[blk, vec],
        out_shape=[jax.ShapeDtypeStruct((t, 2 * SSM_CH), F32), jax.ShapeDtypeStruct((1, 2 * SSM_CH), F32)],
        scratch_shapes=[pltpu.VMEM((8, cb), F32), pltpu.VMEM((8, 2 * cb), F32)],
        sem=("parallel", "arbitrary"), args=(dx, x, x, a), comm=comm)


def _glu_fwd(yc, u, dvec, wg, bg, name, tr=256):
    t = yc.shape[0]

    def body(yc_ref, u_ref, d_ref, w_ref, b_ref, yl_ref, yb_ref):
        yl = yc_ref[...] + d_ref[...] * u_ref[...]
        yl_ref[...] = yl
        yg, _ = _gelu_and_grad(yl)
        z = jnp.dot(yg.astype(BF16), w_ref[...], preferred_element_type=F32) + b_ref[...]
        yb_ref[...] = (yg * _sigmoid(z)).astype(BF16)

    blk = pl.BlockSpec((tr, SSM_WIDTH), lambda i: (i, 0))
    vec = pl.BlockSpec((1, SSM_WIDTH), lambda i: (0, 0))
    return pl.pallas_call(
        body, name=name, grid=(t // tr,),
        in_specs=[blk, blk, vec, pl.BlockSpec((SSM_WIDTH, SSM_WIDTH), lambda i: (0, 0)), vec],
        out_specs=[blk, blk],
        out_shape=[jax.ShapeDtypeStruct((t, SSM_WIDTH), F32), jax.ShapeDtypeStruct((t, SSM_WIDTH), BF16)],
        compiler_params=_params(("parallel",)))(yc, u, dvec, wg, bg)


def _glu_bwd(yl, u, dvec, wg, bg, dyb, name, tr=256):
    t = yl.shape[0]

    def body(yl_ref, u_ref, d_ref, w_ref, b_ref, dy_ref, dyl_ref, du_ref, dw_ref, db_ref, dd_ref):
        @pl.when(pl.program_id(0) == 0)
        def _():
            dw_ref[...] = jnp.zeros_like(dw_ref)
            db_ref[...] = jnp.zeros_like(db_ref)
            dd_ref[...] = jnp.zeros_like(dd_ref)

        ylv, dyv, wv = yl_ref[...], dy_ref[...].astype(F32), w_ref[...]
        yg, dgelu = _gelu_and_grad(ylv)
        ygb = yg.astype(BF16)
        z = jnp.dot(ygb, wv, preferred_element_type=F32) + b_ref[...]
        sg = _sigmoid(z)
        dz = dyv * yg * sg * (1.0 - sg)
        dzb = dz.astype(BF16)
        dyg = dyv * sg + lax.dot_general(dzb, wv, (((1,), (1,)), ((), ())), preferred_element_type=F32)
        dyl = dyg * dgelu
        dyl_ref[...] = dyl.astype(BF16)
        du_ref[...] = dyl * d_ref[...]
        dw_ref[...] += lax.dot_general(ygb, dzb, (((0,), (0,)), ((), ())), preferred_element_type=F32)
        db_ref[...] += jnp.sum(dz, axis=0, keepdims=True)
        dd_ref[...] += jnp.sum(dyl * u_ref[...], axis=0, keepdims=True)

    blk = pl.BlockSpec((tr, SSM_WIDTH), lambda i: (i, 0))
    vec = pl.BlockSpec((1, SSM_WIDTH), lambda i: (0, 0))
    wsp = pl.BlockSpec((SSM_WIDTH, SSM_WIDTH), lambda i: (0, 0))
    return pl.pallas_call(
        body, name=name, grid=(t // tr,), in_specs=[blk, blk, vec, wsp, vec, blk],
        out_specs=[blk, blk, wsp, vec, vec],
        out_shape=[jax.ShapeDtypeStruct((t, SSM_WIDTH), BF16), jax.ShapeDtypeStruct((t, SSM_WIDTH), F32),
                   jax.ShapeDtypeStruct((SSM_WIDTH, SSM_WIDTH), F32), jax.ShapeDtypeStruct((1, SSM_WIDTH), F32),
                   jax.ShapeDtypeStruct((1, SSM_WIDTH), F32)],
        compiler_params=_params(("arbitrary",)))(yl, u, dvec, wg, bg, dyb)


def _mesh_pos():
    return lax.axis_index("x"), lax.axis_index("y"), lax.axis_index("c")


def _device_index():
    x, y, c = _mesh_pos()
    return 4 * x + 2 * y + c


def _gather_comm(arrays):
    na = len(arrays)

    def own_copy(ins, outs, sems, ai):
        return pltpu.make_async_copy(ins[ai], outs[ai].at[_device_index()], sems[2].at[ai])

    def ctx(ins, outs, sems):
        send_sems, recv_sems = sems[:2]
        x, y, c = _mesh_pos()
        chips = [(1 - x, y), (x, 1 - y), (1 - x, 1 - y)]

        def copy(ai, kk, block, to, own=False):
            slot = outs[ai].at[4 * block[0] + 2 * block[1] + block[2]]
            return pltpu.make_async_remote_copy(
                src_ref=ins[ai] if own else slot, dst_ref=slot, send_sem=send_sems.at[ai, kk],
                recv_sem=recv_sems.at[ai, kk], device_id=to, device_id_type=MESH)

        return (x, y, c), (x, y, 1 - c), chips, c, copy

    def start(ins, outs, sems):
        me, sibling, chips, c, copy = ctx(ins, outs, sems)
        for ai in range(na):
            copy(ai, 0, me, sibling, own=True).start()
            for j, chip in enumerate(chips):
                copy(ai, 1 + j, me, (*chip, c), own=True).start()
        for ai in range(na):
            own_copy(ins, outs, sems, ai).start()

    def mid(ins, outs, sems):
        me, sibling, chips, c, copy = ctx(ins, outs, sems)
        for ai in range(na):
            for j, chip in enumerate(chips):
                copy(ai, 1 + j, (*chip, c), me).wait_recv()
                copy(ai, 4 + j, (*chip, c), sibling).start()

    def end(ins, outs, sems):
        me, sibling, chips, c, copy = ctx(ins, outs, sems)
        for ai in range(na):
            copy(ai, 0, sibling, me).wait_recv()
            copy(ai, 0, me, sibling, own=True).wait_send()
            for j, chip in enumerate(chips):
                copy(ai, 4 + j, (*chip, 1 - c), me).wait_recv()
                copy(ai, 1 + j, me, (*chip, c), own=True).wait_send()
                copy(ai, 4 + j, (*chip, c), sibling).wait_send()
            own_copy(ins, outs, sems, ai).wait()

    return Comm(arrays, [jax.ShapeDtypeStruct((N_DEV,) + a.shape, a.dtype) for a in arrays],
                [pltpu.SemaphoreType.DMA((na, 7)), pltpu.SemaphoreType.DMA((na, 7)), pltpu.SemaphoreType.DMA((na,))],
                start, end, mid)


SEQUENCER_GATHER_ID = 7


def _sequencer_gather(arrays, name):
    comm = _gather_comm(arrays)
    ins = [jax.new_ref(a, memory_space=pltpu.MemorySpace.HBM) for a in arrays]
    outs = [jax.empty_ref(s, memory_space=pltpu.MemorySpace.HBM) for s in comm.out_shapes]

    @pl.kernel(mesh=plsc.ScalarSubcoreMesh(axis_name="sequencer", num_cores=1), name=name,
               scratch_types=tuple(comm.sems),
               compiler_params=pltpu.CompilerParams(collective_id=SEQUENCER_GATHER_ID))
    def launch(*sems):
        x, y, c = _mesh_pos()
        peers = [(x, y, 1 - c), (1 - x, y, c), (x, 1 - y, c), (1 - x, 1 - y, c)]
        barrier = pltpu.get_barrier_semaphore()
        for peer in peers:
            pl.semaphore_signal(barrier, inc=1, device_id=peer, device_id_type=MESH)
        pl.semaphore_wait(barrier, len(peers))
        comm.start(ins, outs, sems)
        comm.mid(ins, outs, sems)
        comm.end(ins, outs, sems)

    launch()
    return [o[...] for o in outs]


def _swap_comm(arrays):
    na = len(arrays)
    offs = np.concatenate([[0], np.cumsum([a.shape[1] for a in arrays])]).astype(int)

    def copies(ins, outs, sems):
        x, y, c = _mesh_pos()
        return [pltpu.make_async_remote_copy(
            src_ref=ins[ai].at[2 * k + 1 - c], dst_ref=outs[0].at[k, pl.ds(int(offs[ai]), arrays[ai].shape[1])],
            send_sem=sems[0].at[ai, k], recv_sem=sems[1].at[ai, k], device_id=(x, y, 1 - c), device_id_type=MESH)
            for ai in range(na) for k in range(4)]

    def start(ins, outs, sems):
        for cp in copies(ins, outs, sems):
            cp.start()

    def end(ins, outs, sems):
        for cp in copies(ins, outs, sems):
            cp.wait()

    return Comm(arrays, [jax.ShapeDtypeStruct((4, int(offs[-1]), PACK_COLS), arrays[0].dtype)],
                [pltpu.SemaphoreType.DMA((na, 4)), pltpu.SemaphoreType.DMA((na, 4))], start, end)


def _chips_comm(send):
    def copies(ins, outs, sems):
        x, y, c = _mesh_pos()
        chips = [(1 - x, y), (x, 1 - y), (1 - x, 1 - y)]
        return [pltpu.make_async_remote_copy(
            src_ref=ins[0].at[2 * cx + cy], dst_ref=outs[0].at[j], send_sem=sems[0].at[j], recv_sem=sems[1].at[j],
            device_id=(cx, cy, c), device_id_type=MESH) for j, (cx, cy) in enumerate(chips)]

    def start(ins, outs, sems):
        for cp in copies(ins, outs, sems):
            cp.start()

    def end(ins, outs, sems):
        for cp in copies(ins, outs, sems):
            cp.wait()

    return Comm([send], [jax.ShapeDtypeStruct((3,) + send.shape[1:], send.dtype)],
                [pltpu.SemaphoreType.DMA((3,)), pltpu.SemaphoreType.DMA((3,))], start, end)


def _all_gather(arrays, name):
    na = len(arrays)

    def body(*refs):
        ins, outs = refs[:na], refs[na:2 * na]
        send_sems, recv_sems, local_sems = refs[2 * na:]
        x, y, c = _mesh_pos()
        me, sibling = (x, y, c), (x, y, 1 - c)
        chips = [(1 - x, y), (x, 1 - y), (1 - x, 1 - y)]
        waits = []
        for ai in range(na):
            in_ref, out_ref = ins[ai], outs[ai]

            def slot(px, py, pc, out_ref=out_ref):
                return out_ref.at[4 * px + 2 * py + pc]

            def copy(kk, block, to, src=None, ai=ai, slot=slot):
                return pltpu.make_async_remote_copy(
                    src_ref=slot(*block) if src is None else src, dst_ref=slot(*block),
                    send_sem=send_sems.at[ai, kk], recv_sem=recv_sems.at[ai, kk], device_id=to, device_id_type=MESH)

            mine = pltpu.make_async_copy(in_ref, slot(*me), local_sems.at[ai])
            mine.start()
            first = [copy(0, me, sibling, src=in_ref)]
            first += [copy(1 + j, me, (*chip, c), src=in_ref) for j, chip in enumerate(chips)]
            for cp in first:
                cp.start()
            waits.append((copy, mine, first))
        sends = []
        for ai in range(na):
            copy, mine, first = waits[ai]
            passed = [copy(4 + j, (*chip, c), sibling) for j, chip in enumerate(chips)]
            for j, chip in enumerate(chips):
                copy(1 + j, (*chip, c), me).wait_recv()
                passed[j].start()
            sends.append(passed)
        for ai in range(na):
            copy, mine, first = waits[ai]
            copy(0, sibling, me).wait_recv()
            for j, chip in enumerate(chips):
                copy(4 + j, (*chip, 1 - c), me).wait_recv()
            for cp in first + sends[ai]:
                cp.wait_send()
            mine.wait()

    any_spec = pl.BlockSpec(memory_space=pl.ANY)
    return pl.pallas_call(
        body, name=name, in_specs=[any_spec] * na, out_specs=[any_spec] * na,
        out_shape=[jax.ShapeDtypeStruct((N_DEV,) + a.shape, a.dtype) for a in arrays],
        scratch_shapes=[pltpu.SemaphoreType.DMA((na, 7)), pltpu.SemaphoreType.DMA((na, 7)),
                        pltpu.SemaphoreType.DMA((na,))],
        compiler_params=pltpu.CompilerParams(has_side_effects=True))(*arrays)


def _swap_sibling(arrays, name):
    na = len(arrays)
    offs = np.concatenate([[0], np.cumsum([a.shape[1] for a in arrays])]).astype(int)
    rows = int(offs[-1])

    def body(*refs):
        ins, recv_ref = refs[:na], refs[na]
        send_sems, recv_sems = refs[na + 1:]
        x, y, c = _mesh_pos()
        started = []
        for ai in range(na):
            span = pl.ds(int(offs[ai]), arrays[ai].shape[1])
            for k in range(4):
                remote = pltpu.make_async_remote_copy(
                    src_ref=ins[ai].at[2 * k + 1 - c], dst_ref=recv_ref.at[k, span], send_sem=send_sems.at[ai, k],
                    recv_sem=recv_sems.at[ai, k], device_id=(x, y, 1 - c), device_id_type=MESH)
                remote.start()
                started.append(remote)
        for remote in started:
            remote.wait()

    any_spec = pl.BlockSpec(memory_space=pl.ANY)
    return pl.pallas_call(
        body, name=name, in_specs=[any_spec] * na, out_specs=any_spec,
        out_shape=jax.ShapeDtypeStruct((4, rows, PACK_COLS), arrays[0].dtype),
        scratch_shapes=[pltpu.SemaphoreType.DMA((na, 4)), pltpu.SemaphoreType.DMA((na, 4))])(*arrays)


def _exchange_chips(send, name):
    def body(s_ref, o_ref, send_sems, recv_sems):
        x, y, c = _mesh_pos()
        chips = [(1 - x, y), (x, 1 - y), (1 - x, 1 - y)]
        cps = [pltpu.make_async_remote_copy(
            src_ref=s_ref.at[2 * cx + cy], dst_ref=o_ref.at[j], send_sem=send_sems.at[j], recv_sem=recv_sems.at[j],
            device_id=(cx, cy, c), device_id_type=MESH) for j, (cx, cy) in enumerate(chips)]
        for cp in cps:
            cp.start()
        for cp in cps:
            cp.wait()

    any_spec = pl.BlockSpec(memory_space=pl.ANY)
    return pl.pallas_call(
        body, name=name, in_specs=[any_spec], out_specs=any_spec,
        out_shape=jax.ShapeDtypeStruct((3,) + send.shape[1:], send.dtype),
        scratch_shapes=[pltpu.SemaphoreType.DMA((3,)), pltpu.SemaphoreType.DMA((3,))])(send)


def _pair_sum(keep, recv, name, tr=464):
    nchip, rows, cols = keep.shape

    def body(g_ref, r_ref, o_ref):
        o_ref[...] = (g_ref[...].astype(F32) + r_ref[...].astype(F32)).astype(BF16)

    blk = pl.BlockSpec((1, tr, cols), lambda k, i: (k, i, 0))
    return pl.pallas_call(
        body, name=name, grid=(nchip, rows // tr), in_specs=[blk, blk], out_specs=blk,
        out_shape=jax.ShapeDtypeStruct((nchip, rows, cols), BF16),
        compiler_params=_params(("parallel", "parallel")))(keep, recv)


def _pair_sum_pieces(pieces, recv, name, tr):
    _, rows, cols = pieces.shape
    core = lax.axis_index("c").astype(jnp.int32).reshape(1)

    def body(c_ref, g_ref, r_ref, o_ref):
        del c_ref
        o_ref[...] = (g_ref[...].astype(F32) + r_ref[...].astype(F32)).astype(BF16)

    grid_spec = pltpu.PrefetchScalarGridSpec(
        num_scalar_prefetch=1, grid=(4, rows // tr),
        in_specs=[pl.BlockSpec((1, tr, cols), lambda k, i, c_ref: (2 * k + c_ref[0], i, 0)),
                  pl.BlockSpec((1, tr, cols), lambda k, i, c_ref: (k, i, 0))],
        out_specs=pl.BlockSpec((1, tr, cols), lambda k, i, c_ref: (k, i, 0)))
    return pl.pallas_call(
        body, name=name, grid_spec=grid_spec, out_shape=jax.ShapeDtypeStruct((4, rows, cols), BF16),
        compiler_params=_params(("parallel", "parallel")))(core, pieces, recv)


def _chip_sum(own, others, name, tr=464):
    _, rows, cols = own.shape
    chip = (2 * lax.axis_index("x") + lax.axis_index("y")).astype(jnp.int32).reshape(1)

    def body(chip_ref, own_ref, oth_ref, o_ref):
        del chip_ref
        acc = own_ref[0].astype(F32)
        for j in range(3):
            acc = acc + oth_ref[j].astype(F32)
        o_ref[...] = acc

    grid_spec = pltpu.PrefetchScalarGridSpec(
        num_scalar_prefetch=1, grid=(rows // tr,),
        in_specs=[pl.BlockSpec((1, tr, cols), lambda i, chip_ref: (chip_ref[0], i, 0)),
                  pl.BlockSpec((3, tr, cols), lambda i, chip_ref: (0, i, 0))],
        out_specs=pl.BlockSpec((tr, cols), lambda i, chip_ref: (i, 0)))
    return pl.pallas_call(
        body, name=name, grid_spec=grid_spec, out_shape=jax.ShapeDtypeStruct((rows, cols), F32),
        compiler_params=_params(("parallel",)))(chip, own, others)


def _sum_leading(parts, name, tr=464):
    nparts, rows, cols = parts.shape
    tr = tr if rows % tr == 0 else rows

    def body(p_ref, o_ref):
        acc = p_ref[0].astype(F32)
        for i in range(1, nparts):
            acc = acc + p_ref[i].astype(F32)
        o_ref[...] = acc

    return pl.pallas_call(
        body, name=name, grid=(rows // tr,),
        in_specs=[pl.BlockSpec((nparts, tr, cols), lambda i: (0, i, 0))],
        out_specs=pl.BlockSpec((tr, cols), lambda i: (i, 0)), out_shape=jax.ShapeDtypeStruct((rows, cols), F32),
        compiler_params=_params(("parallel",)))(parts)


def _adamw(w, g, m, v, name, comm=None):
    shape = w.shape
    cols = shape[-1]
    lead = shape[0] if len(shape) >= 3 else 1
    rows = int(np.prod(shape[:-1])) // lead if len(shape) > 1 else 1
    w2, g2, m2, v2 = (a.reshape(lead, rows, cols) for a in (w, g, m, v))
    tr = rows
    for cand in (512, 256, 128, 64, 32, 16, 8):
        if rows % cand == 0 and rows > cand:
            tr = cand
            break
    bc1, bc2 = 1.0 - ADAM_B1 ** ADAM_STEP, 1.0 - ADAM_B2 ** ADAM_STEP

    def body(w_ref, g_ref, m_ref, v_ref, d_ref, nm_ref, nv_ref):
        gv = g_ref[...]
        nm = ADAM_B1 * m_ref[...] + (1.0 - ADAM_B1) * gv
        nv = ADAM_B2 * v_ref[...] + (1.0 - ADAM_B2) * (gv * gv)
        nm_ref[...] = nm
        nv_ref[...] = nv
        d_ref[...] = -ADAM_LR * ((nm / bc1) / (jnp.sqrt(nv / bc2) + ADAM_EPS) + ADAM_WD * w_ref[...])

    blk = pl.BlockSpec((1, tr, cols), lambda l, i: (l, i, 0))
    res = _call(body, name=name, grid=(lead, rows // tr), in_specs=[blk] * 4, out_specs=[blk] * 3,
                out_shape=[jax.ShapeDtypeStruct((lead, rows, cols), F32)] * 3, sem=("parallel", "parallel"),
                args=(w2, g2, m2, v2), comm=comm)
    outs, couts = res if comm is not None else (res, None)
    outs = tuple(o.reshape(shape) for o in outs)
    return outs if comm is None else (outs, couts)


WEIGHT_NAMES = ['norm_mix_g', 'norm_xa_g', 'norm_ffn_g', 'norm_mem_g', 'norm_final_g', 'w_in_ab', 'conv_qkv_a',
                'a_log_a', 'dt_bias_a', 'onorm_g_a', 'ssm_lambda_re', 'ssm_lambda_im', 'ssm_b_re', 'ssm_b_im',
                'ssm_c_re', 'ssm_c_im', 'ssm_d', 'ssm_log_dt', 'w_glu_b', 'b_glu_b', 'w_out_ab', 'pool_w',
                'pool_scale', 'xa_wq', 'xa_wkv', 'xa_wo', 'ffn_w_up', 'ffn_conv', 'ffn_w_down']
BIG_SHARDED = {'w_in_ab': ((1, 1024, 2568), 2), 'w_glu_b': ((1, 512, 512), 1), 'w_out_ab': ((1, 1024, 1024), 1),
               'pool_w': ((1, 4, 256, 256), 2), 'xa_wq': ((2, 1024, 1024), 1), 'xa_wkv': ((2, 1024, 2048), 2),
               'xa_wo': ((2, 1024, 1024), 1), 'ffn_w_up': ((2, 1024, 5632), 2), 'ffn_w_down': ((2, 2816, 1024), 1)}
SMALL_SHARDED = {'conv_qkv_a': ((1, 4, 1536), 2), 'pool_scale': ((1, 1024), 1), 'ffn_conv': ((2, 3, 5632), 2)}
REPLICATED = {'norm_mix_g': (2, 1024), 'norm_xa_g': (2, 1024), 'norm_ffn_g': (2, 1024), 'norm_mem_g': (1024,),
              'norm_final_g': (1024,), 'a_log_a': (1, 4), 'dt_bias_a': (1, 4), 'onorm_g_a': (1, 128),
              'ssm_lambda_re': (1, 32, 64), 'ssm_lambda_im': (1, 32, 64), 'ssm_b_re': (1, 32, 64, 16),
              'ssm_b_im': (1, 32, 64, 16), 'ssm_c_re': (1, 32, 16, 64), 'ssm_c_im': (1, 32, 16, 64),
              'ssm_d': (1, 32, 16), 'ssm_log_dt': (1, 32), 'b_glu_b': (1, 512)}
PACK_ROW_ALIGN = 8


def _shard_shape(shape, axis):
    return tuple(s // N_DEV if i == axis else s for i, s in enumerate(shape))


def _round_up(n, m):
    return (n + m - 1) // m * m


def _pack(arrays):
    total = sum(int(np.prod(a.shape)) for a in arrays)
    padded = _round_up(total, PACK_COLS * PACK_ROW_ALIGN)
    parts = [a.astype(F32).reshape(-1) for a in arrays]
    if padded != total:
        parts.append(jnp.zeros((padded - total,), F32))
    return jnp.concatenate(parts).reshape(padded // PACK_COLS, PACK_COLS)


def _unpack(packed, shapes):
    flat, out, off = packed.reshape(-1), [], 0
    for shape in shapes:
        size = int(np.prod(shape))
        out.append(flat[off:off + size].reshape(shape))
        off += size
    return out


def _split_shards(full, axis):
    shape = full.shape
    s = shape[axis] // N_DEV
    a = full.reshape(shape[:axis] + (N_DEV, s) + shape[axis + 1:])
    return jnp.moveaxis(a, axis, 0).reshape(N_DEV, -1)


def _merge_shards(pieces, shape, axis):
    sh = _shard_shape(shape, axis)
    a = pieces.reshape((N_DEV,) + sh)
    a = jnp.moveaxis(a, 0, axis)
    return a.reshape(shape)


_SCAN_NB = SSM_CH // SCAN_CB


def _to_scan_layout(m, axis):
    shape = m.shape
    m = m.reshape(shape[:axis] + (2, _SCAN_NB, SCAN_CB) + shape[axis + 1:])
    return jnp.swapaxes(m, axis, axis + 1).reshape(shape)


def _from_scan_layout(m, axis):
    shape = m.shape
    m = m.reshape(shape[:axis] + (_SCAN_NB, 2, SCAN_CB) + shape[axis + 1:])
    return jnp.swapaxes(m, axis, axis + 1).reshape(shape)


def _s5_discretise(lam_re, lam_im, b_re, b_im, log_dt):
    dt = jnp.exp(log_dt)[:, None]
    mag = jnp.exp(lam_re * dt)
    ang = lam_im * dt
    lb_re, lb_im = mag * jnp.cos(ang), mag * jnp.sin(ang)
    den = lam_re * lam_re + lam_im * lam_im
    nr, ni = lb_re - 1.0, lb_im
    coef_re = (nr * lam_re + ni * lam_im) / den
    coef_im = (ni * lam_re - nr * lam_im) / den
    bb_re = coef_re[..., None] * b_re - coef_im[..., None] * b_im
    bb_im = coef_re[..., None] * b_im + coef_im[..., None] * b_re
    return lb_re, lb_im, bb_re, bb_im


_GROUPS_PER_BLOCK = N_GROUPS // _SCAN_NB
_U_BLOCK = _GROUPS_PER_BLOCK * SSM_GROUP


def _s5_matrices(lb_re, lb_im, bb_re, bb_im, c_re, c_im):
    eye = jnp.eye(_GROUPS_PER_BLOCK, dtype=F32)
    blocked = lambda m: m.reshape((_SCAN_NB, _GROUPS_PER_BLOCK) + m.shape[1:])
    bmat = lambda bb: jnp.einsum('jgph,gk->jghkp', blocked(bb), eye).reshape(_SCAN_NB, _U_BLOCK, SCAN_CB)
    cmat = lambda cc: jnp.einsum('jghp,gk->jkpgh', blocked(cc), eye).reshape(_SCAN_NB, SCAN_CB, _U_BLOCK)
    b_in = jnp.concatenate([bmat(bb_re), bmat(bb_im)], axis=2)
    c_out = jnp.concatenate([cmat(c_re), -cmat(c_im)], axis=1)
    a_row = _to_scan_layout(jnp.concatenate([lb_re.reshape(1, SSM_CH), lb_im.reshape(1, SSM_CH)], axis=1), 1)
    return b_in, c_out, a_row


def _s5_matrix_grads(db_in, dc_out, da_row):
    da_nat = _from_scan_layout(da_row, 1)
    eye = jnp.eye(_GROUPS_PER_BLOCK, dtype=F32)
    nb, gb = _SCAN_NB, _GROUPS_PER_BLOCK
    bgrad = lambda m: jnp.einsum('jghkp,gk->jgph', m.reshape(nb, gb, SSM_GROUP, gb, SSM_STATE), eye
                                 ).reshape(N_GROUPS, SSM_STATE, SSM_GROUP)
    cgrad = lambda m: jnp.einsum('jkpgh,gk->jghp', m.reshape(nb, gb, SSM_STATE, gb, SSM_GROUP), eye
                                 ).reshape(N_GROUPS, SSM_GROUP, SSM_STATE)
    dbb_re, dbb_im = bgrad(db_in[:, :, :SCAN_CB]), bgrad(db_in[:, :, SCAN_CB:])
    dc_re, dc_im = cgrad(dc_out[:, :SCAN_CB]), -cgrad(dc_out[:, SCAN_CB:])
    dlb_re = da_nat[0, :SSM_CH].reshape(N_GROUPS, SSM_STATE)
    dlb_im = da_nat[0, SSM_CH:].reshape(N_GROUPS, SSM_STATE)
    return dlb_re, dlb_im, dbb_re, dbb_im, dc_re, dc_im


def _as_pieces(a):
    return a.reshape(N_DEV, a.shape[0] // N_DEV, a.shape[1])


def _hybrid_fwd(xn, x, wts, p, weights, riders):
    sv = {}
    hq = _mm(xn, wts['w_qkv_t'], "nt", "l0_in_qkv")
    gate = _mm(xn, wts['w_gate_t'], "nt", "l0_in_gate")
    ba = _mm(xn, wts['w_ba_t'], "nt", "l0_in_ba")
    u = _mm(xn, wts['w_u_t'], "nt", "l0_in_u")
    conv = p['conv_qkv']
    q = _qkv_pre_fwd(hq, conv, 0, 4, True, HEAD_A ** -0.5, "l0_q_pre")
    k = _qkv_pre_fwd(hq, conv, 4, 4, True, 1.0, "l0_k_pre")
    v = _qkv_pre_fwd(hq, conv, 8, 4, False, 1.0, "l0_v_pre")
    gates = _gates_fwd(ba, p['arow'], p['brow'], "l0_gates")
    o, tm_all, s_all = riders.run("l0_gdr_fwd", _gdr_fwd, q, k, v, gates)
    wts['w_glu'], wts['w_out'] = weights.full['w_glu'], weights.full['w_out']
    y_a = _onorm_fwd(o, gate, p['onorm_g'], "l0_onorm")
    bu = riders.run("l0_s5_bu", _mm_bd, u, p['b_in'], "nn")
    xs = riders.run("l0_s5_scan", _s5_scan_fwd, bu, p['a_row'])
    weights.gather_by_sequencer(GATHER_BY_SEQUENCER, after=xs)
    yc = riders.run("l0_s5_cx", _mm_bd, xs, p['c_out'], "nn")
    yl, y_b = _glu_fwd(yc, u, p['d_row'], wts['w_glu'], p['b_glu'], "l0_glu")
    mixed = jnp.concatenate([y_a, y_b], axis=1)
    x1 = _mm(mixed, wts['w_out'], "nn", "l0_out", res=x)
    sv.update(hq=hq, gate=gate, ba=ba, u=u, q=q, k=k, v=v, gb=gates, o=o, tm=tm_all, s=s_all, xs=xs, yl=yl, mixed=mixed)
    return x1, sv


def _hybrid_bwd(dx1, xn, wts, p, sv, riders):
    gr = {}
    dmixed = _mm(dx1, wts['w_out'], "nt", "l0_out_dx", out_dtype=BF16)
    riders.grad('w_out', _as_pieces(_mm(sv['mixed'], dx1, "tn", "l0_out_dw", out_dtype=BF16)))
    dya, dyb = dmixed[:, :WIDTH_A], dmixed[:, WIDTH_A:]
    dyl, du_direct, dw_glu, gr['b_glu_b'], dd = _glu_bwd(
        sv['yl'], sv['u'], p['d_row'], wts['w_glu'], p['b_glu'], dyb, "l0_glu_bwd")
    riders.grad('w_glu', dw_glu.astype(BF16).reshape(N_DEV, -1, PACK_COLS))
    dxs = riders.run("l0_s5_cx_dx", _mm_bd, dyl, p['c_out'], "nt")
    dc_out = _mm_bd(sv['xs'], dyl, "tn", "l0_s5_cx_dw")
    lam, da_row = riders.run("l0_s5_scan_bwd", _s5_scan_bwd, dxs, sv['xs'], p['a_row'])
    du = _mm_bd(lam, p['b_in'], "nt", "l0_s5_bu_dx", res=du_direct, out_dtype=BF16)
    db_in = _mm_bd(sv['u'], lam, "tn", "l0_s5_bu_dw")
    gr['s5'] = (db_in, dc_out, da_row, dd)
    do, dgate, gr['onorm_g_a'] = _onorm_bwd(sv['o'], sv['gate'], p['onorm_g'], dya, "l0_onorm_bwd")
    dq, dk, dv, dgb = riders.run("l0_gdr_bwd", _gdr_bwd, sv['q'], sv['k'], sv['v'], sv['gb'], sv['tm'], sv['s'], do)
    conv = p['conv_qkv']
    dhq_q, dcw_q = _qkv_pre_bwd(sv['hq'], conv, dq, 0, 4, True, HEAD_A ** -0.5, "l0_q_pre_bwd")
    dhq_k, dcw_k = _qkv_pre_bwd(sv['hq'], conv, dk, 4, 4, True, 1.0, "l0_k_pre_bwd")
    dhq_v, dcw_v = _qkv_pre_bwd(sv['hq'], conv, dv, 8, 4, False, 1.0, "l0_v_pre_bwd")
    gr['conv_qkv_a'] = jnp.concatenate([dcw_q, dcw_k, dcw_v], axis=1)
    dhq = jnp.concatenate([dhq_q, dhq_k, dhq_v], axis=1)
    dba, da_log, ddt_bias = _gates_bwd(sv['ba'], p['arow'], p['brow'], dgb, "l0_gates_bwd")
    gr['a_log_a'], gr['dt_bias_a'] = da_log[:, 4:8], ddt_bias[:, 4:8]
    dw_qkv_t = _mm(dhq, xn, "tn", "l0_in_qkv_dw", out_dtype=BF16)
    dw_gate_t = _mm(dgate, xn, "tn", "l0_in_gate_dw", out_dtype=BF16)
    dw_ba_t = _mm(dba, xn, "tn", "l0_in_ba_dw", out_dtype=BF16)
    dw_u_t = _mm(du, xn, "tn", "l0_in_u_dw", out_dtype=BF16)
    dw_in_t = _as_pieces(jnp.concatenate([dw_qkv_t, dw_gate_t, dw_ba_t[:8], dw_u_t], axis=0))
    riders.grad('w_in_t', jnp.concatenate(
        [dw_in_t, jnp.zeros((N_DEV, dict(PIECES)['w_in_t'] - W_IN_PIECE, D_MODEL), BF16)], axis=1))
    dxn = riders.run("l0_in_qkv_dx", _mm, dhq, wts['w_qkv_t'], "nn")
    dxn = _mm(dgate, wts['w_gate_t'], "nn", "l0_in_gate_dx", res=dxn)
    dxn = _mm(dba, wts['w_ba_t'], "nn", "l0_in_ba_dx", res=dxn)
    dxn = _mm(du, wts['w_u_t'], "nn", "l0_in_u_dx", res=dxn)
    return dxn, gr


def _xa_fwd(x1, g, mem_n, wq, wkv_t, wo, tag, riders):
    xq = _rms_fwd(x1, g, BF16, tag + "_norm")
    q = _mm(xq, wq, "nn", tag + "_q", out_dtype=BF16)
    kv = _mm(mem_n, wkv_t, "nt", tag + "_kv", out_dtype=BF16)
    o = riders.run(tag + "_attn", _attn_fwd, q, kv)
    x2 = _mm(o, wo, "nn", tag + "_o", res=x1)
    return x2, dict(xq=xq, q=q, kv=kv, o=o)


def _xa_bwd(dx2, x1, g, mem_n, wq, wkv_t, wo, sv, tag, layer, riders):
    do = _mm(dx2, wo, "nt", tag + "_o_dx", out_dtype=BF16)
    riders.grad('wo%d' % layer, _as_pieces(_mm(sv['o'], dx2, "tn", tag + "_o_dw", out_dtype=BF16)))
    dq, dk, dv = _attn_bwd(sv['q'], sv['kv'], do, tag + "_attn_bwd")
    dkv = jnp.concatenate([dk, dv], axis=1).astype(BF16)
    dxq = _mm(dq, wq, "nt", tag + "_q_dx")
    riders.grad('wq%d' % layer, _as_pieces(_mm(sv['xq'], dq, "tn", tag + "_q_dw", out_dtype=BF16)))
    dmem_n = _mm(dkv, wkv_t, "nn", tag + "_kv_dx")
    riders.grad('wkv_t%d' % layer, _as_pieces(_mm(dkv, mem_n, "tn", tag + "_kv_dw", out_dtype=BF16)))
    dx1, dg = riders.run(tag + "_norm_bwd", _rms_bwd, x1, g, dxq, dx2)
    return dx1, dmem_n, dg


def _ffn_fwd(x2, g, w_up_t, conv, w_down, tag, riders):
    xf = _rms_fwd(x2, g, BF16, tag + "_norm")
    h = riders.run(tag + "_up", _mm, xf, w_up_t, "nt")
    a = riders.run(tag + "_act", _ffn_act_fwd, h, conv)
    x3 = _mm(a, w_down, "nn", tag + "_down", res=x2)
    return x3, dict(xf=xf, h=h, a=a)


def _ffn_bwd(dx3, x2, g, w_up_t, conv, w_down, sv, tag, layer, riders):
    da = _mm(dx3, w_down, "nt", tag + "_down_dx")
    riders.grad('down%d' % layer, _as_pieces(_mm(sv['a'], dx3, "tn", tag + "_down_dw", out_dtype=BF16)))
    dh, dconv = riders.run(tag + "_act_bwd", _ffn_act_bwd, sv['h'], conv, da)
    dxf = riders.run(tag + "_up_dx", _mm, dh, w_up_t, "nn")
    dw_up_t = riders.run(tag + "_up_dw", _mm, dh, sv['xf'], "tn", out_dtype=BF16)
    riders.grad('up_t%d' % layer, _as_pieces(dw_up_t))
    dx2, dg = riders.run(tag + "_norm_bwd", _rms_bwd, x2, g, dxf, dx3)
    return dx2, dconv, dg


BIG_NAMES, SMALL_NAMES, REP_NAMES = list(BIG_SHARDED), list(SMALL_SHARDED), list(REPLICATED)
BIG_SIZES = [int(np.prod(_shard_shape(*BIG_SHARDED[n]))) for n in BIG_NAMES]
SMALL_SIZES = [int(np.prod(_shard_shape(*SMALL_SHARDED[n]))) for n in SMALL_NAMES]


PIECES = [('w_in_t', 384), ('w_glu', 32), ('w_out', 128), ('pool_w', 32), ('wq0', 128), ('wq1', 128),
          ('wkv_t0', 256), ('wkv_t1', 256), ('wo0', 128), ('wo1', 128), ('up_t0', 704), ('up_t1', 704),
          ('down0', 352), ('down1', 352)]
PIECE_OFFS = dict(zip([k for k, _ in PIECES], np.concatenate([[0], np.cumsum([r for _, r in PIECES])[:-1]]).tolist()))
W_IN_ROWS = 4 * WIDTH_A + 2 * N_HEADS_A + SSM_WIDTH
W_IN_PIECE = W_IN_ROWS // N_DEV


def _row_tile(rows):
    return max(t for t in range(16, min(rows, 512) + 1, 16) if rows % t == 0)


class _Riders:
    def __init__(self):
        self.waiting = {}
        self.grads = {}
        self.groups = []
        self.reduced = {}

    def add(self, host, comm, then):
        self.waiting.setdefault(host, []).append((comm, then))

    def run(self, name, fn, *args, **kw):
        riders = self.waiting.pop(name, [])
        if not riders:
            return fn(*args, name=name, **kw)
        out, couts = fn(*args, name=name, comm=[c for c, _ in riders], **kw)
        for (_, then), got in zip(riders, couts):
            then(got)
        return out

    def exchange(self, comm, host, name, then):
        if host is None:
            then(_comm_only(comm, name))
        else:
            self.add(host, comm, then)

    def grad(self, key, pieces):
        self.grads[key] = pieces
        for group in [g for g in self.groups if all(k in self.grads for k in g[1])]:
            self.groups.remove(group)
            self._reduce(*group)

    def _reduce(self, name, keys, swap_host, chips_host):
        arrays = [self.grads[k] for k in keys]
        rows = sum(a.shape[1] for a in arrays)
        tile = _row_tile(rows)

        def after_chips(chip_sums, got):
            total = _chip_sum(chip_sums, got[0], name + "_chip_sum", tr=tile)
            off = 0
            for k, a in zip(keys, arrays):
                self.reduced[k] = total[off:off + a.shape[1]]
                off += a.shape[1]

        def after_swap(got):
            if len(arrays) == 1:
                chip_sums = _pair_sum_pieces(arrays[0], got[0], name + "_pair_sum", tr=tile)
            else:
                core = lax.axis_index("c")
                keep = jnp.concatenate(
                    [lax.dynamic_index_in_dim(a.reshape(4, 2, a.shape[1], PACK_COLS), core, 1, keepdims=False)
                     for a in arrays], axis=1)
                chip_sums = _pair_sum(keep, got[0], name + "_pair_sum", tr=tile)
            self.exchange(_chips_comm(chip_sums), chips_host, name + "_to_chips",
                          functools.partial(after_chips, chip_sums))

        self.exchange(_swap_comm(arrays), swap_host, name + "_to_sibling", after_swap)


class _Weights:
    def __init__(self, inp):
        bf = lambda a: a.astype(BF16)
        local = {'w_in_t': bf(inp['w_in_ab'][0]).T, 'w_glu': bf(inp['w_glu_b'][0]), 'w_out': bf(inp['w_out_ab'][0]),
                 'pool_w': bf(inp['pool_w'][0]),
                 'small': _pack([inp[n] for n in SMALL_NAMES])}
        for l in range(2):
            local['wq%d' % l] = bf(inp['xa_wq'][l])
            local['wkv_t%d' % l] = bf(inp['xa_wkv'][l]).T
            local['wo%d' % l] = bf(inp['xa_wo'][l])
            local['up_t%d' % l] = bf(inp['ffn_w_up'][l]).T
            local['down%d' % l] = bf(inp['ffn_w_down'][l])
        self.local, self.full = local, {}

    def plan(self, keys):
        return _gather_comm([self.local[k] for k in keys])

    def gather_by_sequencer(self, keys, after):
        arrays = [self.local[k] for k in keys]
        tie = (after.reshape(-1)[0] * 0.0).astype(arrays[0].dtype)
        arrays[0] = arrays[0] + tie
        self.land(keys, _sequencer_gather(arrays, "gather_layer1"))

    def land(self, keys, gathered):
        for k, g in zip(keys, gathered):
            if k == 'small':
                off = 0
                for n, size in zip(SMALL_NAMES, SMALL_SIZES):
                    self.full[n] = _merge_shards(g.reshape(N_DEV, -1)[:, off:off + size], *SMALL_SHARDED[n])
                    off += size
            elif k == 'pool_w':
                self.full[k] = jnp.swapaxes(g, 0, 1).reshape(len(POOL_WINDOWS), POOL_GROUP, POOL_GROUP)
            else:
                self.full[k] = g.reshape(N_DEV * g.shape[1], g.shape[2])


GATHER_FIRST = ['w_in_t', 'small']
GATHER_RIDES = [('l0_gdr_fwd', ['w_glu', 'w_out', 'wq0', 'wkv_t0', 'wo0', 'down0']), ('l0_s5_scan', ['up_t0'])]
GATHER_BY_SEQUENCER = ['pool_w', 'wq1', 'wkv_t1', 'wo1', 'up_t1', 'down1']
GRAD_RIDES = [('g_down1', ['down1'], 'l1_ffn_act_bwd', 'l1_ffn_up_dx'),
              ('g_up1', ['up_t1'], 'l1_ffn_norm_bwd', 'l0_ffn_act_bwd'),
              ('g_xa1', ['wq1', 'wkv_t1', 'wo1', 'pool_w'], 'l1_mix_norm_bwd', 'l0_ffn_up_dx'),
              ('g_down0', ['down0'], 'l0_ffn_act_bwd', 'l0_ffn_up_dw'),
              ('g_l0', ['up_t0', 'wq0', 'wkv_t0', 'wo0'], 'l0_xa_norm_bwd', 'l0_gdr_bwd'),
              ('g_out', ['w_out', 'w_glu'], 'l0_s5_cx_dx', 'l0_s5_scan_bwd'),
              ('g_in', ['w_in_t'], 'l0_in_qkv_dx', 'adamw_ffn_w_down')]


def _local_step(inp):
    f32_of = lambda n: inp[n].astype(F32)
    weights = _Weights(inp)
    riders = _Riders()
    riders.groups = list(GRAD_RIDES)
    full = weights.full
    weights.land(GATHER_FIRST, _comm_only(weights.plan(GATHER_FIRST), "gather_first"))
    for host, keys in GATHER_RIDES:
        riders.add(host, weights.plan(keys), functools.partial(weights.land, keys))
    w_in_t = full['w_in_t']
    wts0 = dict(w_qkv_t=w_in_t[:3 * WIDTH_A], w_gate_t=w_in_t[3 * WIDTH_A:4 * WIDTH_A],
                w_ba_t=jnp.concatenate([w_in_t[4 * WIDTH_A:4 * WIDTH_A + 8], jnp.zeros((LANE - 8, D_MODEL), BF16)], 0),
                w_u_t=w_in_t[4 * WIDTH_A + 8:])
    lb_disc, disc_vjp = jax.vjp(_s5_discretise, f32_of('ssm_lambda_re')[0], f32_of('ssm_lambda_im')[0],
                                f32_of('ssm_b_re')[0], f32_of('ssm_b_im')[0], f32_of('ssm_log_dt')[0])
    b_in, c_out, a_row = _s5_matrices(*lb_disc, f32_of('ssm_c_re')[0], f32_of('ssm_c_im')[0])
    zeros4 = jnp.zeros((1, 4), F32)
    p0 = dict(conv_qkv=full['conv_qkv_a'][0], onorm_g=f32_of('onorm_g_a'),
              arow=jnp.concatenate([zeros4, f32_of('a_log_a'), jnp.zeros((1, LANE - 8), F32)], 1),
              brow=jnp.concatenate([zeros4, f32_of('dt_bias_a'), jnp.zeros((1, LANE - 8), F32)], 1),
              b_in=b_in.astype(BF16), c_out=c_out.astype(BF16), a_row=a_row,
              d_row=f32_of('ssm_d').reshape(1, SSM_WIDTH), b_glu=f32_of('b_glu_b'))

    x0 = inp['x'][0]
    mem_n = _rms_fwd(inp['mem'][0], inp['norm_mem_g'], BF16, "mem_norm")
    xn0 = _rms_fwd(x0, inp['norm_mix_g'][0], BF16, "l0_mix_norm")
    x1, sv_mix0 = _hybrid_fwd(xn0, x0, wts0, p0, weights, riders)
    x2, sv_xa0 = _xa_fwd(x1, inp['norm_xa_g'][0], mem_n, full['wq0'], full['wkv_t0'], full['wo0'], "l0_xa", riders)
    x3, sv_ffn0 = _ffn_fwd(x2, inp['norm_ffn_g'][0], full['up_t0'], full['ffn_conv'][0], full['down0'], "l0_ffn", riders)
    xn1 = _rms_fwd(x3, inp['norm_mix_g'][1], F32, "l1_mix_norm")
    x4 = _pool_fwd(xn1, full['pool_w'], full['pool_scale'], x3, "l1_pool")
    x5, sv_xa1 = _xa_fwd(x4, inp['norm_xa_g'][1], mem_n, full['wq1'], full['wkv_t1'], full['wo1'], "l1_xa", riders)
    x6, sv_ffn1 = _ffn_fwd(x5, inp['norm_ffn_g'][1], full['up_t1'], full['ffn_conv'][1], full['down1'], "l1_ffn", riders)
    loss_part, dx6, dg_final = _loss_head(x6, inp['norm_final_g'], inp['loss_target'][0], "loss_head")

    dx5, dconv1, dg_ffn1 = _ffn_bwd(dx6, x5, inp['norm_ffn_g'][1], full['up_t1'], full['ffn_conv'][1], full['down1'],
                                    sv_ffn1, "l1_ffn", 1, riders)
    dx4, dmem1, dg_xa1 = _xa_bwd(dx5, x4, inp['norm_xa_g'][1], mem_n, full['wq1'], full['wkv_t1'], full['wo1'],
                                 sv_xa1, "l1_xa", 1, riders)
    dxn1, dpool_w, dpool_scale = _pool_bwd(xn1, full['pool_w'], full['pool_scale'], dx4, "l1_pool_bwd")
    pool_pieces = jnp.swapaxes(dpool_w.astype(BF16).reshape(len(POOL_WINDOWS), N_DEV, -1, POOL_GROUP), 0, 1)
    riders.grad('pool_w', pool_pieces.reshape(N_DEV, -1, PACK_COLS))
    dx3, dg_mix1 = riders.run("l1_mix_norm_bwd", _rms_bwd, x3, inp['norm_mix_g'][1], dxn1, dx4)
    dx2, dconv0, dg_ffn0 = _ffn_bwd(dx3, x2, inp['norm_ffn_g'][0], full['up_t0'], full['ffn_conv'][0], full['down0'],
                                    sv_ffn0, "l0_ffn", 0, riders)
    dx1, dmem0, dg_xa0 = _xa_bwd(dx2, x1, inp['norm_xa_g'][0], mem_n, full['wq0'], full['wkv_t0'], full['wo0'],
                                 sv_xa0, "l0_xa", 0, riders)
    dxn0, g_mix0 = _hybrid_bwd(dx1, xn0, wts0, p0, sv_mix0, riders)
    grad_x, dg_mix0 = _rms_bwd(x0, inp['norm_mix_g'][0], dxn0, dx1, "l0_mix_norm_bwd")
    _, dg_mem = _rms_bwd(inp['mem'][0], inp['norm_mem_g'], dmem0 + dmem1, None, "mem_norm_bwd")
    assert not riders.groups and all(k.startswith("adamw_") for k in riders.waiting), (list(riders.waiting), riders.groups)

    db_in, dc_out, da_row, dd = g_mix0['s5']
    dlb_re, dlb_im, dbb_re, dbb_im, dc_re, dc_im = _s5_matrix_grads(db_in, dc_out, da_row)
    dlam_re, dlam_im, dbr, dbi, dlog_dt = disc_vjp((dlb_re, dlb_im, dbb_re, dbb_im))

    rep_grads = {
        'norm_mix_g': jnp.concatenate([dg_mix0, dg_mix1], 0), 'norm_xa_g': jnp.concatenate([dg_xa0, dg_xa1], 0),
        'norm_ffn_g': jnp.concatenate([dg_ffn0, dg_ffn1], 0), 'norm_mem_g': dg_mem.reshape(-1),
        'norm_final_g': dg_final.reshape(-1), 'a_log_a': g_mix0['a_log_a'], 'dt_bias_a': g_mix0['dt_bias_a'],
        'onorm_g_a': g_mix0['onorm_g_a'], 'ssm_lambda_re': dlam_re[None], 'ssm_lambda_im': dlam_im[None],
        'ssm_b_re': dbr[None], 'ssm_b_im': dbi[None], 'ssm_c_re': dc_re[None], 'ssm_c_im': dc_im[None],
        'ssm_d': dd.reshape(1, N_GROUPS, SSM_GROUP), 'ssm_log_dt': dlog_dt[None], 'b_glu_b': g_mix0['b_glu_b']}
    small_grads = {'conv_qkv_a': g_mix0['conv_qkv_a'][None], 'pool_scale': dpool_scale,
                   'ffn_conv': jnp.stack([dconv0, dconv1])}
    return loss_part, grad_x, riders, rep_grads, small_grads


ADAMW_ORDER = ['ffn_w_up', 'ffn_w_down', 'xa_wkv', 'xa_wq', 'xa_wo', 'w_out_ab', 'w_glu_b', 'pool_w', 'w_in_ab']
SMALL_GRADS_RIDE_ON = "adamw_ffn_w_up"


def _update(inp, loss_part, grad_x, riders, rep_grads, small_grads):
    dev = _device_index()
    gathered = {}
    misc_local = _pack([rep_grads[n] for n in REP_NAMES] + [small_grads[n] for n in SMALL_NAMES] + [loss_part])
    riders.add(SMALL_GRADS_RIDE_ON, _gather_comm([misc_local]), lambda got: gathered.update(misc=got[0]))
    piece = lambda key: riders.reduced[key]
    both = lambda name: jnp.stack([piece(name + '0'), piece(name + '1')])
    swap = lambda a: jnp.swapaxes(a, -1, -2)
    reduced = {'w_in_ab': lambda: piece('w_in_t')[:W_IN_PIECE][None],
               'w_glu_b': lambda: piece('w_glu').reshape(inp['w_glu_b'].shape),
               'w_out_ab': lambda: piece('w_out')[None], 'pool_w': lambda: piece('pool_w').reshape(inp['pool_w'].shape),
               'xa_wq': lambda: both('wq'), 'xa_wkv': lambda: both('wkv_t'), 'xa_wo': lambda: both('wo'),
               'ffn_w_up': lambda: both('up_t'), 'ffn_w_down': lambda: both('down')}
    transposed = ('w_in_ab', 'xa_wkv', 'ffn_w_up')
    grads, upd = {}, {}
    assert sorted(ADAMW_ORDER) == sorted(BIG_NAMES)
    for n in ADAMW_ORDER:
        fix = swap if n in transposed else (lambda a: a)
        g = reduced[n]()
        out = riders.run("adamw_" + n, _adamw, fix(inp[n]), g, fix(inp['m_' + n]), fix(inp['v_' + n]))
        upd[n], grads[n] = tuple(fix(o) for o in out), fix(g)
    assert not riders.waiting, list(riders.waiting)
    misc_sum = _sum_leading(gathered['misc'], "small_grads_sum")
    misc = _unpack(misc_sum, [inp[n].shape for n in REP_NAMES] + [SMALL_SHARDED[n][0] for n in SMALL_NAMES] + [()])
    loss = misc.pop()
    for n, g in zip(REP_NAMES, misc):
        grads[n] = g
    for n, g in zip(SMALL_NAMES, misc[len(REP_NAMES):]):
        grads[n] = lax.dynamic_index_in_dim(_split_shards(g, SMALL_SHARDED[n][1]), dev, 0, keepdims=False
                                            ).reshape(inp[n].shape)
    tiny_names = REP_NAMES + SMALL_NAMES
    rep_total = sum(int(np.prod(inp[n].shape)) for n in REP_NAMES)
    packs = [_pack([inp[prefix + n] for n in tiny_names]) for prefix in ('', 'm_', 'v_')]
    g_pack = _pack([misc_sum.reshape(-1)[:rep_total]] + [grads[n] for n in SMALL_NAMES])
    tiny_out = [_unpack(o, [inp[n].shape for n in tiny_names])
                for o in _adamw(packs[0], g_pack, packs[1], packs[2], "adamw_small")]
    for i, n in enumerate(tiny_names):
        upd[n] = tuple(o[i] for o in tiny_out)

    outs = [loss, grad_x[None]]
    outs += [grads[n] for n in WEIGHT_NAMES]
    for i in range(3):
        outs += [upd[n][i] for n in WEIGHT_NAMES]
    return tuple(outs)


def _step(inp):
    loss_part, grad_x, riders, rep_grads, small_grads = _local_step(inp)
    return _update(inp, loss_part, grad_x, riders, rep_grads, small_grads)


INPUT_NAMES = (['x', 'mem'] + WEIGHT_NAMES + ['loss_target'] + ['m_' + n for n in WEIGHT_NAMES]
               + ['v_' + n for n in WEIGHT_NAMES])


def kernel(x, mem, norm_mix_g, norm_xa_g, norm_ffn_g, norm_mem_g, norm_final_g, w_in_ab, conv_qkv_a, a_log_a, dt_bias_a, onorm_g_a, ssm_lambda_re, ssm_lambda_im, ssm_b_re, ssm_b_im, ssm_c_re, ssm_c_im, ssm_d, ssm_log_dt, w_glu_b, b_glu_b, w_out_ab, pool_w, pool_scale, xa_wq, xa_wkv, xa_wo, ffn_w_up, ffn_conv, ffn_w_down, loss_target, m_norm_mix_g, m_norm_xa_g, m_norm_ffn_g, m_norm_mem_g, m_norm_final_g, m_w_in_ab, m_conv_qkv_a, m_a_log_a, m_dt_bias_a, m_onorm_g_a, m_ssm_lambda_re, m_ssm_lambda_im, m_ssm_b_re, m_ssm_b_im, m_ssm_c_re, m_ssm_c_im, m_ssm_d, m_ssm_log_dt, m_w_glu_b, m_b_glu_b, m_w_out_ab, m_pool_w, m_pool_scale, m_xa_wq, m_xa_wkv, m_xa_wo, m_ffn_w_up, m_ffn_conv, m_ffn_w_down, v_norm_mix_g, v_norm_xa_g, v_norm_ffn_g, v_norm_mem_g, v_norm_final_g, v_w_in_ab, v_conv_qkv_a, v_a_log_a, v_dt_bias_a, v_onorm_g_a, v_ssm_lambda_re, v_ssm_lambda_im, v_ssm_b_re, v_ssm_b_im, v_ssm_c_re, v_ssm_c_im, v_ssm_d, v_ssm_log_dt, v_w_glu_b, v_b_glu_b, v_w_out_ab, v_pool_w, v_pool_scale, v_xa_wq, v_xa_wkv, v_xa_wo, v_ffn_w_up, v_ffn_conv, v_ffn_w_down):
    args = (x, mem, norm_mix_g, norm_xa_g, norm_ffn_g, norm_mem_g, norm_final_g, w_in_ab, conv_qkv_a, a_log_a, dt_bias_a, onorm_g_a, ssm_lambda_re, ssm_lambda_im, ssm_b_re, ssm_b_im, ssm_c_re, ssm_c_im, ssm_d, ssm_log_dt, w_glu_b, b_glu_b, w_out_ab, pool_w, pool_scale, xa_wq, xa_wkv, xa_wo, ffn_w_up, ffn_conv, ffn_w_down, loss_target, m_norm_mix_g, m_norm_xa_g, m_norm_ffn_g, m_norm_mem_g, m_norm_final_g, m_w_in_ab, m_conv_qkv_a, m_a_log_a, m_dt_bias_a, m_onorm_g_a, m_ssm_lambda_re, m_ssm_lambda_im, m_ssm_b_re, m_ssm_b_im, m_ssm_c_re, m_ssm_c_im, m_ssm_d, m_ssm_log_dt, m_w_glu_b, m_b_glu_b, m_w_out_ab, m_pool_w, m_pool_scale, m_xa_wq, m_xa_wkv, m_xa_wo, m_ffn_w_up, m_ffn_conv, m_ffn_w_down, v_norm_mix_g, v_norm_xa_g, v_norm_ffn_g, v_norm_mem_g, v_norm_final_g, v_w_in_ab, v_conv_qkv_a, v_a_log_a, v_dt_bias_a, v_onorm_g_a, v_ssm_lambda_re, v_ssm_lambda_im, v_ssm_b_re, v_ssm_b_im, v_ssm_c_re, v_ssm_c_im, v_ssm_d, v_ssm_log_dt, v_w_glu_b, v_b_glu_b, v_w_out_ab, v_pool_w, v_pool_scale, v_xa_wq, v_xa_wkv, v_xa_wo, v_ffn_w_up, v_ffn_conv, v_ffn_w_down)
    return _step(dict(zip(INPUT_NAMES, args)))
```

```python
import functools
import math

import numpy as np
import jax
import jax.numpy as jnp
from jax import lax
from jax.experimental import pallas as pl
from jax.experimental.pallas import tpu as pltpu
from jax.experimental.pallas import tpu_sc as plsc

F32, BF16 = jnp.float32, jnp.bfloat16
HIGH, HIGHEST = lax.Precision.HIGH, lax.Precision.HIGHEST
MESH = pl.DeviceIdType.MESH

N_DEV = 8
SEQ, D_MODEL, MEM_LEN = 2048, 1024, 256
WIDTH_A, N_HEADS_A, HEAD_A, CONV_A = 512, 4, 128, 4
GDR_CHUNK = 128
GDR_HEADS_PER_STEP = 4
SSM_WIDTH, SSM_GROUP, N_GROUPS, SSM_STATE = 512, 16, 32, 64
SSM_CH = N_GROUPS * SSM_STATE
SCAN_CB = 512
POOL_WINDOWS = (2, 4, 8, 16)
POOL_GROUP = 256
N_HEADS_X, HEAD_X = 4, 256
D_FF, CONV_FFN = 2816, 3
RMS_EPS = 1e-6
ADAM_LR, ADAM_B1, ADAM_B2, ADAM_EPS, ADAM_WD, ADAM_STEP = 0.001, 0.9, 0.999, 1e-08, 0.01, 10
LANE = 128
PACK_COLS = 1024
VMEM_LIMIT_BYTES = 56 * 1024 * 1024


def _params(sem=None):
    return pltpu.CompilerParams(dimension_semantics=sem, vmem_limit_bytes=VMEM_LIMIT_BYTES)


class Comm:
    def __init__(self, inputs, out_shapes, sems, start, end, mid=None):
        self.inputs, self.out_shapes, self.sems = list(inputs), list(out_shapes), list(sems)
        self.start, self.mid, self.end = start, mid, end


def _merge_comms(comms):
    comms = [c for c in comms if c is not None]
    if not comms:
        return None, []
    bounds, ni, no, ns = [], 0, 0, 0
    for c in comms:
        bounds.append((ni, no, ns))
        ni, no, ns = ni + len(c.inputs), no + len(c.out_shapes), ns + len(c.sems)

    def phase(which):
        def run(ins, outs, sems):
            for c, (i0, o0, s0) in zip(comms, bounds):
                fn = getattr(c, which)
                if fn is not None:
                    fn(ins[i0:i0 + len(c.inputs)], outs[o0:o0 + len(c.out_shapes)], sems[s0:s0 + len(c.sems)])
        return run

    merged = Comm([a for c in comms for a in c.inputs], [s for c in comms for s in c.out_shapes],
                  [s for c in comms for s in c.sems], phase("start"), phase("end"), phase("mid"))
    return merged, [(o0, o0 + len(c.out_shapes)) for c, (_, o0, _) in zip(comms, bounds)]


def _call(body, *, name, grid, in_specs, out_specs, out_shape, args, scratch_shapes=(), sem=None, comm=None):
    single = not isinstance(out_shape, (list, tuple))
    out_specs_l = [out_specs] if single else list(out_specs)
    out_shape_l = [out_shape] if single else list(out_shape)
    scratch_shapes = list(scratch_shapes)
    merged, spans = _merge_comms(comm if isinstance(comm, (list, tuple)) else [comm])
    if merged is None:
        outs = pl.pallas_call(body, name=name, grid=grid, in_specs=list(in_specs), out_specs=out_specs_l,
                              out_shape=out_shape_l, scratch_shapes=scratch_shapes, compiler_params=_params(sem))(*args)
        outs = outs[0] if single else outs
        return outs if comm is None else (outs, [])
    n_in, n_out, n_scr = len(in_specs), len(out_specs_l), len(scratch_shapes)
    ci, co = len(merged.inputs), len(merged.out_shapes)
    total = int(np.prod(grid))

    def wrapped(*refs):
        ins, cins = refs[:n_in], refs[n_in:n_in + ci]
        outs, couts = refs[n_in + ci:n_in + ci + n_out], refs[n_in + ci + n_out:n_in + ci + n_out + co]
        scr, csems = refs[n_in + ci + n_out + co:n_in + ci + n_out + co + n_scr], refs[n_in + ci + n_out + co + n_scr:]
        lin = pl.program_id(0)
        for d in range(1, len(grid)):
            lin = lin * grid[d] + pl.program_id(d)
        pl.when(lin == 0)(lambda: merged.start(cins, couts, csems))
        body(*ins, *outs, *scr)
        mid_step = min((3 * total) // 4, total - 1)
        pl.when(lin == mid_step)(lambda: merged.mid(cins, couts, csems))
        pl.when(lin == total - 1)(lambda: merged.end(cins, couts, csems))

    any_spec = pl.BlockSpec(memory_space=pl.ANY)
    res = pl.pallas_call(
        wrapped, name=name, grid=grid, in_specs=list(in_specs) + [any_spec] * ci,
        out_specs=out_specs_l + [any_spec] * co, out_shape=out_shape_l + merged.out_shapes,
        scratch_shapes=scratch_shapes + merged.sems,
        compiler_params=_params(("arbitrary",) * len(grid)))(*args, *merged.inputs)
    outs, couts = res[:n_out], res[n_out:]
    return (outs[0] if single else list(outs)), [list(couts[a:b]) for a, b in spans]


def _comm_only(comm, name):
    def body():
        pass

    _, couts = _call(body, name=name, grid=(1,), in_specs=[], out_specs=[], out_shape=[], args=[], comm=comm)
    return couts[0]


def _tile(dim, pref):
    best = None
    for t in range(LANE, min(dim, pref) + 1, LANE):
        if dim % t == 0:
            best = t
    return best if best is not None else dim


MM_VMEM_BUDGET = 40 * 1024 * 1024


def _mm_tiles(m, n, k, a_bytes, b_bytes, o_bytes, r_bytes):
    for tk in (k, _tile(k, 2048), _tile(k, 1024), _tile(k, 512)):
        for tm, tn in ((1024, 1536), (1024, 1024), (1024, 512), (512, 512), (256, 512), (256, 256)):
            tm, tn = _tile(m, tm), _tile(n, tn)
            acc = 0 if tk == k else tm * tn * 4
            need = 2 * (tm * tk * a_bytes + tk * tn * b_bytes + tm * tn * (o_bytes + r_bytes)) + acc
            if need <= MM_VMEM_BUDGET:
                return tm, tn, tk
    raise ValueError("no matmul tiling fits VMEM")


def _mm(a, b, mode, name, out_dtype=F32, res=None, comm=None):
    if mode == "nn":
        (m, k), n = a.shape, b.shape[1]
    elif mode == "nt":
        (m, k), n = a.shape, b.shape[0]
    else:
        (k, m), n = a.shape, b.shape[1]
    tm, tn, tk = _mm_tiles(m, n, k, a.dtype.itemsize, b.dtype.itemsize, jnp.dtype(out_dtype).itemsize,
                           0 if res is None else res.dtype.itemsize)
    nk = k // tk
    dims = {"nn": ((1,), (0,)), "nt": ((1,), (1,)), "tn": ((0,), (0,))}[mode]

    def body(*refs):
        if res is None:
            a_ref, b_ref, o_ref = refs[:3]
            r_ref = None
        else:
            a_ref, b_ref, r_ref, o_ref = refs[:4]
        part = lax.dot_general(a_ref[...].astype(BF16), b_ref[...].astype(BF16), (dims, ((), ())),
                               preferred_element_type=F32)

        def finish(out):
            if r_ref is not None:
                out = out + r_ref[...].astype(F32)
            o_ref[...] = out.astype(out_dtype)

        if nk == 1:
            finish(part)
            return
        acc = refs[-1]
        kk = pl.program_id(2)

        @pl.when(kk == 0)
        def _():
            acc[...] = part

        @pl.when(kk > 0)
        def _():
            acc[...] += part

        @pl.when(kk == nk - 1)
        def _():
            finish(acc[...])

    a_spec = (pl.BlockSpec((tk, tm), lambda i, j, q: (q, i)) if mode == "tn"
              else pl.BlockSpec((tm, tk), lambda i, j, q: (i, q)))
    b_spec = (pl.BlockSpec((tn, tk), lambda i, j, q: (j, q)) if mode == "nt"
              else pl.BlockSpec((tk, tn), lambda i, j, q: (q, j)))
    o_spec = pl.BlockSpec((tm, tn), lambda i, j, q: (i, j))
    in_specs, args = [a_spec, b_spec], [a, b]
    if res is not None:
        in_specs.append(o_spec)
        args.append(res)
    return _call(body, name=name, grid=(m // tm, n // tn, nk), in_specs=in_specs, out_specs=o_spec,
                 out_shape=jax.ShapeDtypeStruct((m, n), out_dtype),
                 scratch_shapes=[] if nk == 1 else [pltpu.VMEM((tm, tn), F32)],
                 sem=("parallel", "parallel", "arbitrary"), args=args, comm=comm)


def _mm_bd(a, b, mode, name, out_dtype=F32, res=None, comm=None, tm=1024):
    if mode == "tn":
        k = a.shape[0]
        nb = min(a.shape[1], b.shape[1]) // LANE
        ma, n = a.shape[1] // nb, b.shape[1] // nb

        def body(a_ref, b_ref, o_ref):
            o_ref[0] = lax.dot_general(a_ref[...].astype(BF16), b_ref[...].astype(BF16), (((0,), (0,)), ((), ())),
                                       preferred_element_type=F32).astype(out_dtype)

        return _call(body, name=name, grid=(nb,),
                     in_specs=[pl.BlockSpec((k, ma), lambda j: (0, j)), pl.BlockSpec((k, n), lambda j: (0, j))],
                     out_specs=pl.BlockSpec((1, ma, n), lambda j: (j, 0, 0)),
                     out_shape=jax.ShapeDtypeStruct((nb, ma, n), out_dtype), sem=("parallel",), args=(a, b), comm=comm)
    m = a.shape[0]
    nb = b.shape[0]
    ka = a.shape[1] // nb
    n = b.shape[2] if mode == "nn" else b.shape[1]
    tm = _tile(m, tm)
    dims = ((1,), (0,)) if mode == "nn" else ((1,), (1,))

    def body(*refs):
        if res is None:
            a_ref, b_ref, o_ref = refs
            r_ref = None
        else:
            a_ref, b_ref, r_ref, o_ref = refs
        out = lax.dot_general(a_ref[...].astype(BF16), b_ref[0].astype(BF16), (dims, ((), ())),
                              preferred_element_type=F32)
        if r_ref is not None:
            out = out + r_ref[...].astype(F32)
        o_ref[...] = out.astype(out_dtype)

    o_spec = pl.BlockSpec((tm, n), lambda i, j: (i, j))
    in_specs = [pl.BlockSpec((tm, ka), lambda i, j: (i, j)), pl.BlockSpec((1,) + b.shape[1:], lambda i, j: (j, 0, 0))]
    args = [a, b]
    if res is not None:
        in_specs.append(o_spec)
        args.append(res)
    return _call(body, name=name, grid=(m // tm, nb), in_specs=in_specs, out_specs=o_spec,
                 out_shape=jax.ShapeDtypeStruct((m, nb * n), out_dtype), sem=("parallel", "parallel"),
                 args=args, comm=comm)


def _rms_fwd(x, g, out_dtype, name, tr=256):
    rows, d = x.shape

    def body(x_ref, g_ref, o_ref):
        xv = x_ref[...]
        r = lax.rsqrt(jnp.mean(xv * xv, axis=-1, keepdims=True) + RMS_EPS)
        o_ref[...] = (xv * r * g_ref[...]).astype(out_dtype)

    return pl.pallas_call(
        body, name=name, grid=(rows // tr,),
        in_specs=[pl.BlockSpec((tr, d), lambda i: (i, 0)), pl.BlockSpec((1, d), lambda i: (0, 0))],
        out_specs=pl.BlockSpec((tr, d), lambda i: (i, 0)), out_shape=jax.ShapeDtypeStruct((rows, d), out_dtype),
        compiler_params=_params(("parallel",)))(x, g.reshape(1, d))


def _rms_bwd(x, g, dy, dres, name, tr=256, comm=None):
    rows, d = x.shape

    def body(*refs):
        if dres is None:
            x_ref, g_ref, dy_ref, dx_ref, dg_ref = refs
            r_ref = None
        else:
            x_ref, g_ref, dy_ref, r_ref, dx_ref, dg_ref = refs

        @pl.when(pl.program_id(0) == 0)
        def _():
            dg_ref[...] = jnp.zeros_like(dg_ref)

        xv, dyv = x_ref[...], dy_ref[...].astype(F32)
        r = lax.rsqrt(jnp.mean(xv * xv, axis=-1, keepdims=True) + RMS_EPS)
        xh = xv * r
        dyg = dyv * g_ref[...]
        dx = r * (dyg - xh * jnp.mean(dyg * xh, axis=-1, keepdims=True))
        if r_ref is not None:
            dx = dx + r_ref[...]
        dx_ref[...] = dx
        dg_ref[...] += jnp.sum(dyv * xh, axis=0, keepdims=True)

    blk = pl.BlockSpec((tr, d), lambda i: (i, 0))
    vec = pl.BlockSpec((1, d), lambda i: (0, 0))
    in_specs, args = [blk, vec, blk], [x, g.reshape(1, d), dy]
    if dres is not None:
        in_specs.append(blk)
        args.append(dres)
    return _call(
        body, name=name, grid=(rows // tr,), in_specs=in_specs, out_specs=[blk, vec],
        out_shape=[jax.ShapeDtypeStruct((rows, d), F32), jax.ShapeDtypeStruct((1, d), F32)],
        sem=("arbitrary",), args=args, comm=comm)


def _loss_head(x, g, target, name, tr=256):
    rows, d = x.shape

    def body(x_ref, g_ref, t_ref, loss_ref, dx_ref, dg_ref):
        @pl.when(pl.program_id(0) == 0)
        def _():
            dg_ref[...] = jnp.zeros_like(dg_ref)
            loss_ref[...] = jnp.zeros_like(loss_ref)

        xv = x_ref[...]
        r = lax.rsqrt(jnp.mean(xv * xv, axis=-1, keepdims=True) + RMS_EPS)
        xh = xv * r
        err = xh * g_ref[...] - t_ref[...]
        loss_ref[...] += 0.5 * jnp.sum(jnp.mean(err * err, axis=-1, keepdims=True), keepdims=True)
        dyv = err * (1.0 / d)
        dyg = dyv * g_ref[...]
        dx_ref[...] = r * (dyg - xh * jnp.mean(dyg * xh, axis=-1, keepdims=True))
        dg_ref[...] += jnp.sum(dyv * xh, axis=0, keepdims=True)

    blk = pl.BlockSpec((tr, d), lambda i: (i, 0))
    vec = pl.BlockSpec((1, d), lambda i: (0, 0))
    return pl.pallas_call(
        body, name=name, grid=(rows // tr,), in_specs=[blk, vec, blk],
        out_specs=[pl.BlockSpec((1, 1), lambda i: (0, 0)), blk, vec],
        out_shape=[jax.ShapeDtypeStruct((1, 1), F32), jax.ShapeDtypeStruct((rows, d), F32),
                   jax.ShapeDtypeStruct((1, d), F32)],
        compiler_params=_params(("arbitrary",)))(x, g.reshape(1, d), target)


def _shift_down(x, s):
    rows = lax.broadcasted_iota(jnp.int32, x.shape, 0)
    return jnp.where(rows >= s, pltpu.roll(x, s, 0), 0.0)


def _shift_up(x, s):
    n = x.shape[0]
    rows = lax.broadcasted_iota(jnp.int32, x.shape, 0)
    return jnp.where(rows < n - s, pltpu.roll(x, n - s, 0), 0.0)


def _sigmoid(x):
    return 1.0 / (1.0 + jnp.exp(-x))


def _silu_and_grad(x):
    s = _sigmoid(x)
    return x * s, s * (1.0 + x * (1.0 - s))


_GELU_C0, _GELU_C1 = math.sqrt(2.0 / math.pi), 0.044715


def _gelu_and_grad(x):
    th = jnp.tanh(_GELU_C0 * (x + _GELU_C1 * x * x * x))
    y = 0.5 * x * (1.0 + th)
    dy = 0.5 * (1.0 + th) + 0.5 * x * (1.0 - th * th) * _GELU_C0 * (1.0 + 3.0 * _GELU_C1 * x * x)
    return y, dy


def _ffn_act_fwd(h, w, name, tc=256, comm=None):
    t = h.shape[0]
    nb = D_FF // tc

    def body(hg_ref, hv_ref, wg_ref, wv_ref, a_ref):
        def conv(x, wr):
            return wr[2:3, :] * x + wr[1:2, :] * _shift_down(x, 1) + wr[0:1, :] * _shift_down(x, 2)

        cg = conv(hg_ref[...], wg_ref[...])
        cv = conv(hv_ref[...], wv_ref[...])
        a_ref[...] = (cg * _sigmoid(cg) * cv).astype(BF16)

    return _call(
        body, name=name, grid=(nb,),
        in_specs=[pl.BlockSpec((t, tc), lambda j: (0, j)), pl.BlockSpec((t, tc), lambda j: (0, j + nb)),
                  pl.BlockSpec((CONV_FFN, tc), lambda j: (0, j)), pl.BlockSpec((CONV_FFN, tc), lambda j: (0, j + nb))],
        out_specs=pl.BlockSpec((t, tc), lambda j: (0, j)), out_shape=jax.ShapeDtypeStruct((t, D_FF), BF16),
        sem=("parallel",), args=(h, h, w, w), comm=comm)


def _ffn_act_bwd(h, w, da, name, tc=256, comm=None):
    t = h.shape[0]
    nb = D_FF // tc

    def body(hg_ref, hv_ref, wg_ref, wv_ref, da_ref, dhg_ref, dhv_ref, dwg_ref, dwv_ref):
        hg, hv, wg, wv = hg_ref[...], hv_ref[...], wg_ref[...], wv_ref[...]
        hg1, hg2, hv1, hv2 = _shift_down(hg, 1), _shift_down(hg, 2), _shift_down(hv, 1), _shift_down(hv, 2)
        cg = wg[2:3, :] * hg + wg[1:2, :] * hg1 + wg[0:1, :] * hg2
        cv = wv[2:3, :] * hv + wv[1:2, :] * hv1 + wv[0:1, :] * hv2
        sg, dsg = _silu_and_grad(cg)
        dav = da_ref[...].astype(F32)
        dcv = dav * sg
        dcg = dav * cv * dsg

        def conv_t(dc, wr):
            return wr[2:3, :] * dc + wr[1:2, :] * _shift_up(dc, 1) + wr[0:1, :] * _shift_up(dc, 2)

        dhg_ref[...] = conv_t(dcg, wg).astype(BF16)
        dhv_ref[...] = conv_t(dcv, wv).astype(BF16)
        dwg_ref[0:1, :] = jnp.sum(dcg * hg2, axis=0, keepdims=True)
        dwg_ref[1:2, :] = jnp.sum(dcg * hg1, axis=0, keepdims=True)
        dwg_ref[2:3, :] = jnp.sum(dcg * hg, axis=0, keepdims=True)
        dwv_ref[0:1, :] = jnp.sum(dcv * hv2, axis=0, keepdims=True)
        dwv_ref[1:2, :] = jnp.sum(dcv * hv1, axis=0, keepdims=True)
        dwv_ref[2:3, :] = jnp.sum(dcv * hv, axis=0, keepdims=True)

    big = lambda off: pl.BlockSpec((t, tc), lambda j: (0, j + off))
    small = lambda off: pl.BlockSpec((CONV_FFN, tc), lambda j: (0, j + off))
    res = _call(
        body, name=name, grid=(nb,),
        in_specs=[big(0), big(nb), small(0), small(nb), big(0)],
        out_specs=[big(0), big(0), small(0), small(0)],
        out_shape=[jax.ShapeDtypeStruct((t, D_FF), BF16), jax.ShapeDtypeStruct((t, D_FF), BF16),
                   jax.ShapeDtypeStruct((CONV_FFN, D_FF), F32), jax.ShapeDtypeStruct((CONV_FFN, D_FF), F32)],
        sem=("parallel",), args=(h, h, w, w, da), comm=comm)
    (dhg, dhv, dwg, dwv), couts = res if comm is not None else (res, None)
    out = (jnp.concatenate([dhg, dhv], axis=1), jnp.concatenate([dwg, dwv], axis=1))
    return out if comm is None else (out, couts)


def _attn_probs(q, k):
    s = lax.dot_general(q.astype(BF16), k.astype(BF16), (((1,), (1,)), ((), ())),
                        preferred_element_type=F32) * (HEAD_X ** -0.5)
    s = s - jnp.max(s, axis=-1, keepdims=True)
    p = jnp.exp(s)
    return p / jnp.sum(p, axis=-1, keepdims=True)


def _attn_fwd(q, kv, name, tq=512, comm=None):
    t = q.shape[0]

    def body(q_ref, k_ref, v_ref, o_ref):
        p = _attn_probs(q_ref[...], k_ref[...])
        o_ref[...] = jnp.dot(p.astype(BF16), v_ref[...].astype(BF16), preferred_element_type=F32).astype(BF16)

    return _call(
        body, name=name, grid=(N_HEADS_X, t // tq),
        in_specs=[pl.BlockSpec((tq, HEAD_X), lambda h, i: (i, h)),
                  pl.BlockSpec((MEM_LEN, HEAD_X), lambda h, i: (0, h)),
                  pl.BlockSpec((MEM_LEN, HEAD_X), lambda h, i: (0, h + N_HEADS_X))],
        out_specs=pl.BlockSpec((tq, HEAD_X), lambda h, i: (i, h)),
        out_shape=jax.ShapeDtypeStruct((t, N_HEADS_X * HEAD_X), BF16),
        sem=("parallel", "parallel"), args=(q, kv, kv), comm=comm)


def _attn_bwd(q, kv, do, name, tq=512):
    t = q.shape[0]

    def body(q_ref, k_ref, v_ref, do_ref, dq_ref, dk_ref, dv_ref):
        @pl.when(pl.program_id(1) == 0)
        def _():
            dk_ref[...] = jnp.zeros_like(dk_ref)
            dv_ref[...] = jnp.zeros_like(dv_ref)

        qb, kb, vb, dob = (r[...].astype(BF16) for r in (q_ref, k_ref, v_ref, do_ref))
        p = _attn_probs(qb, kb)
        dp = lax.dot_general(dob, vb, (((1,), (1,)), ((), ())), preferred_element_type=F32)
        ds = p * (dp - jnp.sum(dp * p, axis=-1, keepdims=True)) * (HEAD_X ** -0.5)
        dsb = ds.astype(BF16)
        dq_ref[...] = jnp.dot(dsb, kb, preferred_element_type=F32).astype(BF16)
        dk_ref[...] += lax.dot_general(dsb, qb, (((0,), (0,)), ((), ())), preferred_element_type=F32)
        dv_ref[...] += lax.dot_general(p.astype(BF16), dob, (((0,), (0,)), ((), ())), preferred_element_type=F32)

    qs = pl.BlockSpec((tq, HEAD_X), lambda h, i: (i, h))
    ms = pl.BlockSpec((MEM_LEN, HEAD_X), lambda h, i: (0, h))
    return pl.pallas_call(
        body, name=name, grid=(N_HEADS_X, t // tq),
        in_specs=[qs, ms, pl.BlockSpec((MEM_LEN, HEAD_X), lambda h, i: (0, h + N_HEADS_X)), qs],
        out_specs=[qs, ms, ms],
        out_shape=[jax.ShapeDtypeStruct((t, D_MODEL), BF16), jax.ShapeDtypeStruct((MEM_LEN, D_MODEL), F32),
                   jax.ShapeDtypeStruct((MEM_LEN, D_MODEL), F32)],
        compiler_params=_params(("parallel", "arbitrary")))(q, kv, kv, do)


def _pool_counts(t, win):
    pos = lax.broadcasted_iota(jnp.int32, (t, 1), 0).astype(F32) + 1.0
    return 1.0 / jnp.minimum(pos, float(win))


def _pool_delta(xv, win):
    s, step = xv, 1
    while step < win:
        s = s + _shift_down(s, step)
        step *= 2
    return s * _pool_counts(xv.shape[0], win) - xv


def _pool_delta_t(dv, win):
    s, step = dv * _pool_counts(dv.shape[0], win), 1
    while step < win:
        s = s + _shift_up(s, step)
        step *= 2
    return s - dv


def _pool_fwd(xn, w, scale, res, name):
    t = xn.shape[0]

    def make_branch(win, xn_ref, w_ref, s_ref, r_ref, o_ref):
        def branch():
            dl = _pool_delta(xn_ref[...], win)
            y = jnp.dot(dl.astype(BF16), w_ref[0], preferred_element_type=F32)
            o_ref[...] = r_ref[...] + y * s_ref[...]
        return branch

    def body(xn_ref, w_ref, s_ref, r_ref, o_ref):
        for gi, win in enumerate(POOL_WINDOWS):
            pl.when(pl.program_id(0) == gi)(make_branch(win, xn_ref, w_ref, s_ref, r_ref, o_ref))

    blk = pl.BlockSpec((t, POOL_GROUP), lambda g: (0, g))
    return pl.pallas_call(
        body, name=name, grid=(len(POOL_WINDOWS),),
        in_specs=[blk, pl.BlockSpec((1, POOL_GROUP, POOL_GROUP), lambda g: (g, 0, 0)),
                  pl.BlockSpec((1, POOL_GROUP), lambda g: (0, g)), blk],
        out_specs=blk, out_shape=jax.ShapeDtypeStruct((t, D_MODEL), F32),
        compiler_params=_params(("parallel",)))(xn, w, scale, res)


def _pool_bwd(xn, w, scale, dmix, name):
    t = xn.shape[0]

    def make_branch(win, xn_ref, w_ref, s_ref, d_ref, dxn_ref, dw_ref, ds_ref):
        def branch():
            dl = _pool_delta(xn_ref[...], win).astype(BF16)
            wv = w_ref[0]
            dm = d_ref[...]
            y = jnp.dot(dl, wv, preferred_element_type=F32)
            ds_ref[...] = jnp.sum(dm * y, axis=0, keepdims=True)
            dy = (dm * s_ref[...]).astype(BF16)
            dw_ref[0] = lax.dot_general(dl, dy, (((0,), (0,)), ((), ())), preferred_element_type=F32)
            ddl = lax.dot_general(dy, wv, (((1,), (1,)), ((), ())), preferred_element_type=F32)
            dxn_ref[...] = _pool_delta_t(ddl, win)
        return branch

    def body(*refs):
        for gi, win in enumerate(POOL_WINDOWS):
            pl.when(pl.program_id(0) == gi)(make_branch(win, *refs))

    blk = pl.BlockSpec((t, POOL_GROUP), lambda g: (0, g))
    wspec = pl.BlockSpec((1, POOL_GROUP, POOL_GROUP), lambda g: (g, 0, 0))
    vec = pl.BlockSpec((1, POOL_GROUP), lambda g: (0, g))
    return pl.pallas_call(
        body, name=name, grid=(len(POOL_WINDOWS),), in_specs=[blk, wspec, vec, blk], out_specs=[blk, wspec, vec],
        out_shape=[jax.ShapeDtypeStruct((t, D_MODEL), F32),
                   jax.ShapeDtypeStruct((len(POOL_WINDOWS), POOL_GROUP, POOL_GROUP), F32),
                   jax.ShapeDtypeStruct((1, D_MODEL), F32)],
        compiler_params=_params(("parallel",)))(xn, w, scale, dmix)


def _qkv_conv(h, wr):
    return (wr[3:4, :] * h + wr[2:3, :] * _shift_down(h, 1) + wr[1:2, :] * _shift_down(h, 2)
            + wr[0:1, :] * _shift_down(h, 3))


def _qkv_pre_fwd(h, w, col0, ncols, normalize, scale, name):
    t = h.shape[0]

    def body(h_ref, w_ref, o_ref):
        c = _qkv_conv(h_ref[...], w_ref[...])
        s = c * _sigmoid(c)
        if normalize:
            s = s * lax.rsqrt(jnp.sum(s * s, axis=-1, keepdims=True) + 1e-6) * scale
        o_ref[...] = s

    return pl.pallas_call(
        body, name=name, grid=(ncols,),
        in_specs=[pl.BlockSpec((t, HEAD_A), lambda j: (0, j + col0)), pl.BlockSpec((CONV_A, HEAD_A), lambda j: (0, j + col0))],
        out_specs=pl.BlockSpec((t, HEAD_A), lambda j: (0, j)), out_shape=jax.ShapeDtypeStruct((t, ncols * HEAD_A), F32),
        compiler_params=_params(("parallel",)))(h, w)


def _qkv_pre_bwd(h, w, dy, col0, ncols, normalize, scale, name):
    t = h.shape[0]

    def body(h_ref, w_ref, dy_ref, dh_ref, dw_ref):
        hv, wr, dyv = h_ref[...], w_ref[...], dy_ref[...]
        h1, h2, h3 = _shift_down(hv, 1), _shift_down(hv, 2), _shift_down(hv, 3)
        c = wr[3:4, :] * hv + wr[2:3, :] * h1 + wr[1:2, :] * h2 + wr[0:1, :] * h3
        s, dsilu = _silu_and_grad(c)
        if normalize:
            r = lax.rsqrt(jnp.sum(s * s, axis=-1, keepdims=True) + 1e-6)
            y = s * r
            dyv = dyv * scale
            ds = r * (dyv - y * jnp.sum(dyv * y, axis=-1, keepdims=True))
        else:
            ds = dyv
        dc = ds * dsilu
        dh = (wr[3:4, :] * dc + wr[2:3, :] * _shift_up(dc, 1) + wr[1:2, :] * _shift_up(dc, 2)
              + wr[0:1, :] * _shift_up(dc, 3))
        dh_ref[...] = dh.astype(BF16)
        dw_ref[0:1, :] = jnp.sum(dc * h3, axis=0, keepdims=True)
        dw_ref[1:2, :] = jnp.sum(dc * h2, axis=0, keepdims=True)
        dw_ref[2:3, :] = jnp.sum(dc * h1, axis=0, keepdims=True)
        dw_ref[3:4, :] = jnp.sum(dc * hv, axis=0, keepdims=True)

    return pl.pallas_call(
        body, name=name, grid=(ncols,),
        in_specs=[pl.BlockSpec((t, HEAD_A), lambda j: (0, j + col0)), pl.BlockSpec((CONV_A, HEAD_A), lambda j: (0, j + col0)),
                  pl.BlockSpec((t, HEAD_A), lambda j: (0, j))],
        out_specs=[pl.BlockSpec((t, HEAD_A), lambda j: (0, j)), pl.BlockSpec((CONV_A, HEAD_A), lambda j: (0, j))],
        out_shape=[jax.ShapeDtypeStruct((t, ncols * HEAD_A), BF16), jax.ShapeDtypeStruct((CONV_A, ncols * HEAD_A), F32)],
        compiler_params=_params(("parallel",)))(h, w, dy)


def _softplus(x):
    return jnp.maximum(x, 0.0) + jnp.log1p(jnp.exp(-jnp.abs(x)))


def _gates_fwd(ba, arow, brow, name):
    t = ba.shape[0]

    def body(x_ref, a_ref, b_ref, o_ref):
        xv = x_ref[...]
        lane = lax.broadcasted_iota(jnp.int32, xv.shape, 1)
        beta = _sigmoid(xv)
        g = -jnp.exp(a_ref[...]) * _softplus(xv + b_ref[...])
        o_ref[...] = jnp.where(lane < N_HEADS_A, beta, jnp.where(lane < 2 * N_HEADS_A, g, 0.0))

    return pl.pallas_call(body, name=name, out_shape=jax.ShapeDtypeStruct((t, LANE), F32),
                          compiler_params=_params())(ba, arow, brow)


def _gates_bwd(ba, arow, brow, dgb, name):
    t = ba.shape[0]

    def body(x_ref, a_ref, b_ref, d_ref, dx_ref, da_ref, db_ref):
        xv = x_ref[...]
        dv = d_ref[0] + d_ref[1] + d_ref[2] + d_ref[3]
        lane = lax.broadcasted_iota(jnp.int32, xv.shape, 1)
        beta = _sigmoid(xv)
        ea = jnp.exp(a_ref[...])
        z = xv + b_ref[...]
        dgv = jnp.where((lane >= N_HEADS_A) & (lane < 2 * N_HEADS_A), dv, 0.0) * (-ea)
        dz = dgv * _sigmoid(z)
        dx = jnp.where(lane < N_HEADS_A, dv * beta * (1.0 - beta), dz)
        dx_ref[...] = dx.astype(BF16)
        db_ref[...] = jnp.sum(dz, axis=0, keepdims=True)
        da_ref[...] = jnp.sum(dgv * _softplus(z), axis=0, keepdims=True)

    return pl.pallas_call(
        body, name=name,
        out_shape=[jax.ShapeDtypeStruct((t, LANE), BF16), jax.ShapeDtypeStruct((1, LANE), F32),
                   jax.ShapeDtypeStruct((1, LANE), F32)],
        compiler_params=_params())(ba, arow, brow, dgb)


def _dot(a, b, prec=None):
    if prec is None:
        return jnp.dot(a.astype(BF16), b.astype(BF16), preferred_element_type=F32)
    return jnp.dot(a, b, precision=prec, preferred_element_type=F32)


def _dot_nt(a, b, prec=None):
    if prec is None:
        a, b = a.astype(BF16), b.astype(BF16)
    return lax.dot_general(a, b, (((1,), (1,)), ((), ())), precision=prec, preferred_element_type=F32)


def _dot_tn(a, b, prec=None):
    if prec is None:
        a, b = a.astype(BF16), b.astype(BF16)
    return lax.dot_general(a, b, (((0,), (0,)), ((), ())), precision=prec, preferred_element_type=F32)


def _gdr_chunk_terms(k, beta, g):
    c = GDR_CHUNK
    row = lax.broadcasted_iota(jnp.int32, (c, c), 0)
    col = lax.broadcasted_iota(jnp.int32, (c, c), 1)
    causal, strict = row >= col, row > col
    gcum = _dot(causal.astype(F32), jnp.broadcast_to(g, (c, c)), HIGHEST)
    diff = gcum - gcum.T
    decay = jnp.where(causal, jnp.exp(jnp.where(causal, diff, 0.0)), 0.0)
    kb = k * beta
    kk = _dot_nt(kb, k)
    return row, col, causal, strict, gcum, decay, kb, kk


def _unit_lower_inverse(a):
    c = a.shape[0]
    eye = (lax.broadcasted_iota(jnp.int32, (c, c), 0) == lax.broadcasted_iota(jnp.int32, (c, c), 1)).astype(F32)
    p = -a
    inv = eye + p
    step = 1
    while 2 * step < c:
        p = _dot(p, p, HIGH)
        inv = inv + _dot(inv, p, HIGH)
        step *= 2
    return inv


def _head_gates(gates, head):
    lane = lax.broadcasted_iota(jnp.int32, gates.shape, 1)
    beta = jnp.sum(jnp.where(lane == head, gates, 0.0), axis=1, keepdims=True)
    g = jnp.sum(jnp.where(lane == head + N_HEADS_A, gates, 0.0), axis=1, keepdims=True)
    return beta, g


def _gdr_fwd(q, k, v, gates, name, comm=None):
    t = q.shape[0]
    c = GDR_CHUNK
    n = t // c

    hps = GDR_HEADS_PER_STEP

    def one_head(hh, q_ref, k_ref, v_ref, gb_ref, o_ref, tm_ref, s_ref, state):
        cols = slice(hh * HEAD_A, (hh + 1) * HEAD_A)
        qv, kv, vv = q_ref[:, cols], k_ref[:, cols], v_ref[:, cols]
        beta, g = _head_gates(gb_ref[...], pl.program_id(0) * hps + hh)
        row, col, causal, strict, gcum, decay, kb, kk = _gdr_chunk_terms(kv, beta, g)
        tm = _unit_lower_inverse(jnp.where(strict, kk * decay, 0.0))
        e = jnp.exp(gcum)
        u = _dot(tm, vv * beta, HIGH)
        w = _dot(tm, kb * e, HIGH)
        p = jnp.where(causal, _dot_nt(qv, kv) * decay, 0.0)
        s = state[hh]
        s_ref[hh, 0] = s
        tm_ref[hh, 0] = tm
        vn = u - _dot(w, s)
        o_ref[:, cols] = _dot(qv * e, s) + _dot(p, vn)
        glast = gcum[c - 1:c, :]
        state[hh] = s * jnp.exp(glast) + _dot_tn(kv * jnp.exp(glast - gcum), vn)

    def body(*refs):
        state = refs[-1]

        @pl.when(pl.program_id(1) == 0)
        def _():
            state[...] = jnp.zeros_like(state)

        for hh in range(hps):
            one_head(hh, *refs)

    blk = pl.BlockSpec((c, hps * HEAD_A), lambda h, i: (i, h))
    mat = pl.BlockSpec((hps, 1, c, c), lambda h, i: (h, i, 0, 0))
    return _call(
        body, name=name, grid=(N_HEADS_A // hps, n),
        in_specs=[blk, blk, blk, pl.BlockSpec((c, LANE), lambda h, i: (i, 0))],
        out_specs=[blk, mat, mat],
        out_shape=[jax.ShapeDtypeStruct((t, WIDTH_A), F32), jax.ShapeDtypeStruct((N_HEADS_A, n, c, c), F32),
                   jax.ShapeDtypeStruct((N_HEADS_A, n, HEAD_A, HEAD_A), F32)],
        scratch_shapes=[pltpu.VMEM((hps, HEAD_A, HEAD_A), F32)], sem=("parallel", "arbitrary"),
        args=(q, k, v, gates), comm=comm)


def _gdr_bwd(q, k, v, gates, tm_all, s_all, do, name, comm=None):
    t = q.shape[0]
    c = GDR_CHUNK
    n = t // c

    hps = GDR_HEADS_PER_STEP

    def one_head(hh, q_ref, k_ref, v_ref, gb_ref, tm_ref, s_ref, do_ref, dq_ref, dk_ref, dv_ref, dgb_ref, dstate):
        cols = slice(hh * HEAD_A, (hh + 1) * HEAD_A)
        qv, kv, vv, dov = q_ref[:, cols], k_ref[:, cols], v_ref[:, cols], do_ref[:, cols]
        head = pl.program_id(0) * hps + hh
        beta, g = _head_gates(gb_ref[...], head)
        tm, s, dsp = tm_ref[hh, 0], s_ref[hh, 0], dstate[hh]
        row, col, causal, strict, gcum, decay, kb, kk = _gdr_chunk_terms(kv, beta, g)
        e = jnp.exp(gcum)
        vb, kbe = vv * beta, kb * e
        u = _dot(tm, vb, HIGH)
        w = _dot(tm, kbe, HIGH)
        qk = _dot_nt(qv, kv)
        p = jnp.where(causal, qk * decay, 0.0)
        vn = u - _dot(w, s)
        glast = gcum[c - 1:c, :]
        el = jnp.exp(glast)
        f = jnp.exp(glast - gcum)
        kd = kv * f
        qe = qv * e

        dvn = _dot_tn(p, dov) + _dot(kd, dsp)
        dglast = el[:, 0:1] * jnp.sum(s * dsp, keepdims=True)
        dkd = _dot_nt(vn, dsp)
        dk = dkd * f
        df = jnp.sum(dkd * kv, axis=1, keepdims=True) * f[:, 0:1]
        dglast = dglast + jnp.sum(df, keepdims=True)
        dgc = -df
        dp = jnp.where(causal, _dot_nt(dov, vn), 0.0)
        dqe = _dot_nt(dov, s)
        dq = dqe * e
        de = jnp.sum(dqe * qv, axis=1, keepdims=True)
        dstate[hh] = dsp * el + _dot_tn(qe, dov) - _dot_tn(w, dvn)
        dw = -_dot_nt(dvn, s)
        dvb = _dot_tn(tm, dvn, HIGH)
        dkbe = _dot_tn(tm, dw, HIGH)
        da = -jnp.where(strict, _dot_nt(dvb, u) + _dot_nt(dkbe, w), 0.0)
        dkk = da * decay
        dqk = dp * decay
        dd = da * kk + dp * qk
        dq = dq + _dot(dqk, kv)
        dk = dk + _dot_tn(dqk, qv)
        dkb = _dot(dkk, kv) + dkbe * e
        dk = dk + _dot_tn(dkk, kb)
        de = de + jnp.sum(dkbe * kb, axis=1, keepdims=True)
        dk = dk + dkb * beta
        dbeta = jnp.sum(dkb * kv, axis=1, keepdims=True) + jnp.sum(dvb * vv, axis=1, keepdims=True)
        m = dd * decay
        dgc = dgc + jnp.sum(m, axis=1, keepdims=True) - jnp.sum(m.T, axis=1, keepdims=True)
        dgc = dgc + de * e[:, 0:1]
        dgc = dgc + jnp.where(row[:, 0:1] == c - 1, dglast, 0.0)
        dg = _dot((row <= col).astype(F32), jnp.broadcast_to(dgc, (c, c)), HIGHEST)
        dq_ref[:, cols] = dq
        dk_ref[:, cols] = dk
        dv_ref[:, cols] = dvb * beta
        lane = lax.broadcasted_iota(jnp.int32, (c, LANE), 1)
        dgb_ref[hh] = jnp.where(lane == head, dbeta, jnp.where(lane == head + N_HEADS_A, dg, 0.0))

    def body(*refs):
        dstate = refs[-1]

        @pl.when(pl.program_id(1) == 0)
        def _():
            dstate[...] = jnp.zeros_like(dstate)

        for hh in range(hps):
            one_head(hh, *refs)

    blk = pl.BlockSpec((c, hps * HEAD_A), lambda h, i: (n - 1 - i, h))
    mat = pl.BlockSpec((hps, 1, c, c), lambda h, i: (h, n - 1 - i, 0, 0))
    return _call(
        body, name=name, grid=(N_HEADS_A // hps, n),
        in_specs=[blk, blk, blk, pl.BlockSpec((c, LANE), lambda h, i: (n - 1 - i, 0)), mat, mat, blk],
        out_specs=[blk, blk, blk, pl.BlockSpec((hps, c, LANE), lambda h, i: (h, n - 1 - i, 0))],
        out_shape=[jax.ShapeDtypeStruct((t, WIDTH_A), F32)] * 3 + [jax.ShapeDtypeStruct((N_HEADS_A, t, LANE), F32)],
        scratch_shapes=[pltpu.VMEM((hps, HEAD_A, HEAD_A), F32)], sem=("parallel", "arbitrary"),
        args=(q, k, v, gates, tm_all, s_all, do), comm=comm)


_B_NN, _B_NT, _B_TN = ((2,), (1,)), ((2,), (2,)), ((1,), (1,))


def _bdot(a, b, dims=_B_NN, prec=None):
    if prec is None:
        a, b = a.astype(BF16), b.astype(BF16)
    return lax.dot_general(a, b, (dims, ((0,), (0,))), precision=prec, preferred_element_type=F32)


def _heads_of(ref):
    return jnp.stack([ref[:, h * HEAD_A:(h + 1) * HEAD_A] for h in range(N_HEADS_A)])


def _all_head_gates(gates):
    pairs = [_head_gates(gates, h) for h in range(N_HEADS_A)]
    return jnp.stack([b for b, _ in pairs]), jnp.stack([g for _, g in pairs])


def _gdr_terms(k, beta, g):
    h, c = k.shape[0], GDR_CHUNK
    row = lax.broadcasted_iota(jnp.int32, (c, c), 0)
    col = lax.broadcasted_iota(jnp.int32, (c, c), 1)
    causal, strict = row >= col, row > col
    lower = jnp.broadcast_to(causal.astype(F32), (h, c, c))
    gcum = _bdot(lower, jnp.broadcast_to(g, (h, c, c)), prec=HIGHEST)
    diff = gcum - jnp.swapaxes(gcum, 1, 2)
    decay = jnp.where(causal, jnp.exp(jnp.where(causal, diff, 0.0)), 0.0)
    kb = k * beta
    return row, col, causal, strict, gcum, decay, kb, _bdot(kb, k, _B_NT)


def _unit_lower_inverses(a):
    c = a.shape[1]
    eye = (lax.broadcasted_iota(jnp.int32, (c, c), 0) == lax.broadcasted_iota(jnp.int32, (c, c), 1)).astype(F32)
    p = -a
    inv = eye + p
    step = 1
    while 2 * step < c:
        p = _bdot(p, p, prec=HIGH)
        inv = inv + _bdot(inv, p, prec=HIGH)
        step *= 2
    return inv


def _gdr_fwd(q, k, v, gates, name, comm=None):
    t = q.shape[0]
    c, nh = GDR_CHUNK, N_HEADS_A
    n = t // c

    def body(q_ref, k_ref, v_ref, gb_ref, o_ref, tm_ref, s_ref, state):
        @pl.when(pl.program_id(0) == 0)
        def _():
            state[...] = jnp.zeros_like(state)

        qv, kv, vv = _heads_of(q_ref), _heads_of(k_ref), _heads_of(v_ref)
        beta, g = _all_head_gates(gb_ref[...])
        row, col, causal, strict, gcum, decay, kb, kk = _gdr_terms(kv, beta, g)
        tm = _unit_lower_inverses(jnp.where(strict, kk * decay, 0.0))
        e = jnp.exp(gcum)
        u = _bdot(tm, vv * beta, prec=HIGH)
        w = _bdot(tm, kb * e, prec=HIGH)
        p = jnp.where(causal, _bdot(qv, kv, _B_NT) * decay, 0.0)
        s = state[...]
        s_ref[:, 0] = s
        tm_ref[:, 0] = tm
        vn = u - _bdot(w, s)
        o = _bdot(qv * e, s) + _bdot(p, vn)
        for h in range(nh):
            o_ref[:, h * HEAD_A:(h + 1) * HEAD_A] = o[h]
        glast = gcum[:, c - 1:c, :]
        state[...] = s * jnp.exp(glast) + _bdot(kv * jnp.exp(glast - gcum), vn, _B_TN)

    blk = pl.BlockSpec((c, WIDTH_A), lambda i: (i, 0))
    mat = pl.BlockSpec((nh, 1, c, c), lambda i: (0, i, 0, 0))
    return _call(
        body, name=name, grid=(n,), in_specs=[blk, blk, blk, pl.BlockSpec((c, LANE), lambda i: (i, 0))],
        out_specs=[blk, mat, mat],
        out_shape=[jax.ShapeDtypeStruct((t, WIDTH_A), F32), jax.ShapeDtypeStruct((nh, n, c, c), F32),
                   jax.ShapeDtypeStruct((nh, n, HEAD_A, HEAD_A), F32)],
        scratch_shapes=[pltpu.VMEM((nh, HEAD_A, HEAD_A), F32)], sem=("arbitrary",),
        args=(q, k, v, gates), comm=comm)


def _gdr_bwd(q, k, v, gates, tm_all, s_all, do, name, comm=None):
    t = q.shape[0]
    c, nh = GDR_CHUNK, N_HEADS_A
    n = t // c

    def body(q_ref, k_ref, v_ref, gb_ref, tm_ref, s_ref, do_ref, dq_ref, dk_ref, dv_ref, dgb_ref, dstate):
        @pl.when(pl.program_id(0) == 0)
        def _():
            dstate[...] = jnp.zeros_like(dstate)

        qv, kv, vv, dov = _heads_of(q_ref), _heads_of(k_ref), _heads_of(v_ref), _heads_of(do_ref)
        beta, g = _all_head_gates(gb_ref[...])
        tm, s, dsp = tm_ref[:, 0], s_ref[:, 0], dstate[...]
        row, col, causal, strict, gcum, decay, kb, kk = _gdr_terms(kv, beta, g)
        rowsum = lambda x: jnp.sum(x, axis=2, keepdims=True)
        e = jnp.exp(gcum)
        vb, kbe = vv * beta, kb * e
        u = _bdot(tm, vb, prec=HIGH)
        w = _bdot(tm, kbe, prec=HIGH)
        qk = _bdot(qv, kv, _B_NT)
        p = jnp.where(causal, qk * decay, 0.0)
        vn = u - _bdot(w, s)
        glast = gcum[:, c - 1:c, :]
        el = jnp.exp(glast)
        f = jnp.exp(glast - gcum)
        kd = kv * f
        qe = qv * e

        dvn = _bdot(p, dov, _B_TN) + _bdot(kd, dsp)
        dglast = el[:, :, 0:1] * jnp.sum(s * dsp, axis=(1, 2), keepdims=True)
        dkd = _bdot(vn, dsp, _B_NT)
        dk = dkd * f
        df = rowsum(dkd * kv) * f[:, :, 0:1]
        dglast = dglast + jnp.sum(df, axis=1, keepdims=True)
        dgc = -df
        dp = jnp.where(causal, _bdot(dov, vn, _B_NT), 0.0)
        dqe = _bdot(dov, s, _B_NT)
        dq = dqe * e
        de = rowsum(dqe * qv)
        dstate[...] = dsp * el + _bdot(qe, dov, _B_TN) - _bdot(w, dvn, _B_TN)
        dw = -_bdot(dvn, s, _B_NT)
        dvb = _bdot(tm, dvn, _B_TN, prec=HIGH)
        dkbe = _bdot(tm, dw, _B_TN, prec=HIGH)
        da = -jnp.where(strict, _bdot(dvb, u, _B_NT) + _bdot(dkbe, w, _B_NT), 0.0)
        dkk = da * decay
        dqk = dp * decay
        dd = da * kk + dp * qk
        dq = dq + _bdot(dqk, kv)
        dk = dk + _bdot(dqk, qv, _B_TN)
        dkb = _bdot(dkk, kv) + dkbe * e
        dk = dk + _bdot(dkk, kb, _B_TN)
        de = de + rowsum(dkbe * kb)
        dk = dk + dkb * beta
        dbeta = rowsum(dkb * kv) + rowsum(dvb * vv)
        m = dd * decay
        dgc = dgc + rowsum(m) - rowsum(jnp.swapaxes(m, 1, 2))
        dgc = dgc + de * e[:, :, 0:1]
        dgc = dgc + jnp.where(row[:, 0:1] == c - 1, dglast, 0.0)
        upper = jnp.broadcast_to((row <= col).astype(F32), (nh, c, c))
        dg = _bdot(upper, jnp.broadcast_to(dgc, (nh, c, c)), prec=HIGHEST)
        dv = dvb * beta
        for h in range(nh):
            cols = slice(h * HEAD_A, (h + 1) * HEAD_A)
            dq_ref[:, cols] = dq[h]
            dk_ref[:, cols] = dk[h]
            dv_ref[:, cols] = dv[h]
        head = lax.broadcasted_iota(jnp.int32, (nh, c, LANE), 0)
        lane = lax.broadcasted_iota(jnp.int32, (nh, c, LANE), 2)
        dgb_ref[...] = jnp.where(lane == head, dbeta, jnp.where(lane == head + nh, dg, 0.0))

    blk = pl.BlockSpec((c, WIDTH_A), lambda i: (n - 1 - i, 0))
    mat = pl.BlockSpec((nh, 1, c, c), lambda i: (0, n - 1 - i, 0, 0))
    return _call(
        body, name=name, grid=(n,),
        in_specs=[blk, blk, blk, pl.BlockSpec((c, LANE), lambda i: (n - 1 - i, 0)), mat, mat, blk],
        out_specs=[blk, blk, blk, pl.BlockSpec((nh, c, LANE), lambda i: (0, n - 1 - i, 0))],
        out_shape=[jax.ShapeDtypeStruct((t, WIDTH_A), F32)] * 3 + [jax.ShapeDtypeStruct((nh, t, LANE), F32)],
        scratch_shapes=[pltpu.VMEM((nh, HEAD_A, HEAD_A), F32)], sem=("arbitrary",),
        args=(q, k, v, gates, tm_all, s_all, do), comm=comm)


def _onorm_fwd(o, gate, g, name):
    t = o.shape[0]

    def body(o_ref, gate_ref, g_ref, y_ref):
        ov, gv = o_ref[...], gate_ref[...]
        r = lax.rsqrt(jnp.mean(ov * ov, axis=-1, keepdims=True) + RMS_EPS)
        y_ref[...] = (ov * r * g_ref[...] * gv * _sigmoid(gv)).astype(BF16)

    blk = pl.BlockSpec((t, HEAD_A), lambda j: (0, j))
    return pl.pallas_call(
        body, name=name, grid=(N_HEADS_A,), in_specs=[blk, blk, pl.BlockSpec((1, HEAD_A), lambda j: (0, 0))],
        out_specs=blk, out_shape=jax.ShapeDtypeStruct((t, WIDTH_A), BF16),
        compiler_params=_params(("parallel",)))(o, gate, g)


def _onorm_bwd(o, gate, g, dy, name):
    t = o.shape[0]

    def body(o_ref, gate_ref, g_ref, dy_ref, do_ref, dgate_ref, dg_ref):
        @pl.when(pl.program_id(0) == 0)
        def _():
            dg_ref[...] = jnp.zeros_like(dg_ref)

        ov, gv, dyv = o_ref[...], gate_ref[...], dy_ref[...].astype(F32)
        r = lax.rsqrt(jnp.mean(ov * ov, axis=-1, keepdims=True) + RMS_EPS)
        oh = ov * r
        sg, dsg = _silu_and_grad(gv)
        dgate_ref[...] = (dyv * oh * g_ref[...] * dsg).astype(BF16)
        dn = dyv * sg
        dg_ref[...] += jnp.sum(dn * oh, axis=0, keepdims=True)
        dng = dn * g_ref[...]
        do_ref[...] = r * (dng - oh * jnp.mean(dng * oh, axis=-1, keepdims=True))

    blk = pl.BlockSpec((t, HEAD_A), lambda j: (0, j))
    vec = pl.BlockSpec((1, HEAD_A), lambda j: (0, 0))
    return pl.pallas_call(
        body, name=name, grid=(N_HEADS_A,), in_specs=[blk, blk, vec, blk], out_specs=[blk, blk, vec],
        out_shape=[jax.ShapeDtypeStruct((t, WIDTH_A), F32), jax.ShapeDtypeStruct((t, WIDTH_A), BF16),
                   jax.ShapeDtypeStruct((1, HEAD_A), F32)],
        compiler_params=_params(("arbitrary",)))(o, gate, g, dy)


def _cmul(ar, ai, br, bi):
    return ar * br - ai * bi, ar * bi + ai * br


def _scan_tables(ar, ai, reverse):
    p1 = (ar, ai)
    p2 = _cmul(*p1, *p1)
    p4 = _cmul(*p2, *p2)
    p8 = _cmul(*p4, *p4)
    p3 = _cmul(*p2, *p1)
    p5 = _cmul(*p4, *p1)
    p6 = _cmul(*p4, *p2)
    p7 = _cmul(*p4, *p3)
    pows = [p1, p2, p3, p4, p5, p6, p7, p8]
    rows = lax.broadcasted_iota(jnp.int32, (8, ar.shape[1]), 0)
    tr = jnp.zeros((8, ar.shape[1]), F32)
    ti = jnp.zeros((8, ar.shape[1]), F32)
    for r in range(8):
        pw = pows[7 - r] if reverse else pows[r]
        tr = jnp.where(rows == r, pw[0], tr)
        ti = jnp.where(rows == r, pw[1], ti)
    return p1, p2, p4, p8, tr, ti


def _tile_scan(xr, xi, p1, p2, p4, reverse):
    rows = lax.broadcasted_iota(jnp.int32, xr.shape, 0)
    for s, (pr, pi) in ((1, p1), (2, p2), (4, p4)):
        if reverse:
            keep = rows < 8 - s
            sr, si = pltpu.roll(xr, 8 - s, 0), pltpu.roll(xi, 8 - s, 0)
        else:
            keep = rows >= s
            sr, si = pltpu.roll(xr, s, 0), pltpu.roll(xi, s, 0)
        sr, si = jnp.where(keep, sr, 0.0), jnp.where(keep, si, 0.0)
        mr, mi = _cmul(pr, pi, sr, si)
        xr, xi = xr + mr, xi + mi
    return xr, xi


def _s5_scan_fwd(bu, a, name, tb=512, comm=None):
    t = bu.shape[0]
    cb = SCAN_CB
    nt = t // tb

    def body(b_ref, a_ref, x_ref, carry):
        @pl.when(pl.program_id(1) == 0)
        def _():
            carry[...] = jnp.zeros_like(carry)

        ar, ai = a_ref[:, 0:cb], a_ref[:, cb:2 * cb]
        p1, p2, p4, p8, tr, ti = _scan_tables(ar, ai, False)

        def step(j, c):
            cr, ci = c
            i = pl.multiple_of(j * 8, 8)
            xr, xi = _tile_scan(b_ref[pl.ds(i, 8), 0:cb], b_ref[pl.ds(i, 8), cb:2 * cb], p1, p2, p4, False)
            mr, mi = _cmul(tr, ti, cr, ci)
            xr, xi = xr + mr, xi + mi
            x_ref[pl.ds(i, 8), 0:cb] = xr
            x_ref[pl.ds(i, 8), cb:2 * cb] = xi
            return xr[7:8, :], xi[7:8, :]

        cr, ci = lax.fori_loop(0, tb // 8, step, (carry[0:1, :], carry[1:2, :]), unroll=2)
        carry[0:1, :] = cr
        carry[1:2, :] = ci

    blk = pl.BlockSpec((tb, 2 * cb), lambda j, i: (i, j))
    return _call(
        body, name=name, grid=(SSM_CH // cb, nt),
        in_specs=[blk, pl.BlockSpec((1, 2 * cb), lambda j, i: (0, j))], out_specs=blk,
        out_shape=jax.ShapeDtypeStruct((t, 2 * SSM_CH), F32), scratch_shapes=[pltpu.VMEM((8, cb), F32)],
        sem=("parallel", "arbitrary"), args=(bu, a), comm=comm)


def _s5_scan_bwd(dx, x, a, name, tb=512, comm=None):
    t = dx.shape[0]
    cb = SCAN_CB
    nt = t // tb
    nj = tb // 8

    def body(d_ref, x_ref, xp_ref, a_ref, l_ref, da_ref, carry, acc):
        tblk = pl.program_id(1)

        @pl.when(tblk == 0)
        def _():
            carry[...] = jnp.zeros_like(carry)
            acc[...] = jnp.zeros_like(acc)

        ar, ai = a_ref[:, 0:cb], a_ref[:, cb:2 * cb]
        p1, p2, p4, p8, tr, ti = _scan_tables(ar, -ai, True)
        rows = lax.broadcasted_iota(jnp.int32, (8, cb), 0)

        def step(jj, c):
            cr, ci, sr_acc, si_acc = c
            j = nj - 1 - jj
            i = pl.multiple_of(j * 8, 8)
            lr, li = _tile_scan(d_ref[pl.ds(i, 8), 0:cb], d_ref[pl.ds(i, 8), cb:2 * cb], p1, p2, p4, True)
            mr, mi = _cmul(tr, ti, cr, ci)
            lr, li = lr + mr, li + mi
            l_ref[pl.ds(i, 8), 0:cb] = lr
            l_ref[pl.ds(i, 8), cb:2 * cb] = li
            ip = pl.multiple_of(jnp.maximum(j - 1, 0) * 8, 8)
            prev_r = jnp.where(j > 0, x_ref[pl.ds(ip, 8), 0:cb], xp_ref[:, 0:cb])
            prev_i = jnp.where(j > 0, x_ref[pl.ds(ip, 8), cb:2 * cb], xp_ref[:, cb:2 * cb])
            edge = jnp.where(jnp.logical_and(j == 0, tblk == nt - 1), 0.0, 1.0)
            xs_r = jnp.where(rows == 0, pltpu.roll(prev_r, 1, 0) * edge, pltpu.roll(x_ref[pl.ds(i, 8), 0:cb], 1, 0))
            xs_i = jnp.where(rows == 0, pltpu.roll(prev_i, 1, 0) * edge, pltpu.roll(x_ref[pl.ds(i, 8), cb:2 * cb], 1, 0))
            sr_acc = sr_acc + lr * xs_r + li * xs_i
            si_acc = si_acc + li * xs_r - lr * xs_i
            return lr[0:1, :], li[0:1, :], sr_acc, si_acc

        cr, ci, sr_acc, si_acc = lax.fori_loop(
            0, nj, step, (carry[0:1, :], carry[1:2, :], acc[:, 0:cb], acc[:, cb:2 * cb]))
        carry[0:1, :] = cr
        carry[1:2, :] = ci
        acc[:, 0:cb] = sr_acc
        acc[:, cb:2 * cb] = si_acc

        @pl.when(tblk == nt - 1)
        def _():
            da_ref[...] = jnp.sum(acc[...], axis=0, keepdims=True)

    blk = pl.BlockSpec((tb, 2 * cb), lambda j, i: (nt - 1 - i, j))
    prev = pl.BlockSpec((8, 2 * cb), lambda j, i: (jnp.maximum((nt - 1 - i) * (tb // 8) - 1, 0), j))
    vec = pl.BlockSpec((1, 2 * cb), lambda j, i: (0, j))
    return _call(
        body, name=name, grid=(SSM_CH // cb, nt), in_specs=[blk, blk, prev, vec], out_specs=[blk, vec],
        out_shape=[jax.ShapeDtypeStruct((t, 2 * SSM_CH), F32), jax.ShapeDtypeStruct((1, 2 * SSM_CH), F32)],
        scratch_shapes=[pltpu.VMEM((8, cb), F32), pltpu.VMEM((8, 2 * cb), F32)],
        sem=("parallel", "arbitrary"), args=(dx, x, x, a), comm=comm)


def _glu_fwd(yc, u, dvec, wg, bg, name, tr=256):
    t = yc.shape[0]

    def body(yc_ref, u_ref, d_ref, w_ref, b_ref, yl_ref, yb_ref):
        yl = yc_ref[...] + d_ref[...] * u_ref[...]
        yl_ref[...] = yl
        yg, _ = _gelu_and_grad(yl)
        z = jnp.dot(yg.astype(BF16), w_ref[...], preferred_element_type=F32) + b_ref[...]
        yb_ref[...] = (yg * _sigmoid(z)).astype(BF16)

    blk = pl.BlockSpec((tr, SSM_WIDTH), lambda i: (i, 0))
    vec = pl.BlockSpec((1, SSM_WIDTH), lambda i: (0, 0))
    return pl.pallas_call(
        body, name=name, grid=(t // tr,),
        in_specs=[blk, blk, vec, pl.BlockSpec((SSM_WIDTH, SSM_WIDTH), lambda i: (0, 0)), vec],
        out_specs=[blk, blk],
        out_shape=[jax.ShapeDtypeStruct((t, SSM_WIDTH), F32), jax.ShapeDtypeStruct((t, SSM_WIDTH), BF16)],
        compiler_params=_params(("parallel",)))(yc, u, dvec, wg, bg)


def _glu_bwd(yl, u, dvec, wg, bg, dyb, name, tr=256):
    t = yl.shape[0]

    def body(yl_ref, u_ref, d_ref, w_ref, b_ref, dy_ref, dyl_ref, du_ref, dw_ref, db_ref, dd_ref):
        @pl.when(pl.program_id(0) == 0)
        def _():
            dw_ref[...] = jnp.zeros_like(dw_ref)
            db_ref[...] = jnp.zeros_like(db_ref)
            dd_ref[...] = jnp.zeros_like(dd_ref)

        ylv, dyv, wv = yl_ref[...], dy_ref[...].astype(F32), w_ref[...]
        yg, dgelu = _gelu_and_grad(ylv)
        ygb = yg.astype(BF16)
        z = jnp.dot(ygb, wv, preferred_element_type=F32) + b_ref[...]
        sg = _sigmoid(z)
        dz = dyv * yg * sg * (1.0 - sg)
        dzb = dz.astype(BF16)
        dyg = dyv * sg + lax.dot_general(dzb, wv, (((1,), (1,)), ((), ())), preferred_element_type=F32)
        dyl = dyg * dgelu
        dyl_ref[...] = dyl.astype(BF16)
        du_ref[...] = dyl * d_ref[...]
        dw_ref[...] += lax.dot_general(ygb, dzb, (((0,), (0,)), ((), ())), preferred_element_type=F32)
        db_ref[...] += jnp.sum(dz, axis=0, keepdims=True)
        dd_ref[...] += jnp.sum(dyl * u_ref[...], axis=0, keepdims=True)

    blk = pl.BlockSpec((tr, SSM_WIDTH), lambda i: (i, 0))
    vec = pl.BlockSpec((1, SSM_WIDTH), lambda i: (0, 0))
    wsp = pl.BlockSpec((SSM_WIDTH, SSM_WIDTH), lambda i: (0, 0))
    return pl.pallas_call(
        body, name=name, grid=(t // tr,), in_specs=[blk, blk, vec, wsp, vec, blk],
        out_specs=[blk, blk, wsp, vec, vec],
        out_shape=[jax.ShapeDtypeStruct((t, SSM_WIDTH), BF16), jax.ShapeDtypeStruct((t, SSM_WIDTH), F32),
                   jax.ShapeDtypeStruct((SSM_WIDTH, SSM_WIDTH), F32), jax.ShapeDtypeStruct((1, SSM_WIDTH), F32),
                   jax.ShapeDtypeStruct((1, SSM_WIDTH), F32)],
        compiler_params=_params(("arbitrary",)))(yl, u, dvec, wg, bg, dyb)


def _mesh_pos():
    return lax.axis_index("x"), lax.axis_index("y"), lax.axis_index("c")


def _device_index():
    x, y, c = _mesh_pos()
    return 4 * x + 2 * y + c


def _gather_comm(arrays):
    na = len(arrays)

    def own_copy(ins, outs, sems, ai):
        return pltpu.make_async_copy(ins[ai], outs[ai].at[_device_index()], sems[2].at[ai])

    def ctx(ins, outs, sems):
        send_sems, recv_sems = sems[:2]
        x, y, c = _mesh_pos()
        chips = [(1 - x, y), (x, 1 - y), (1 - x, 1 - y)]

        def copy(ai, kk, block, to, own=False):
            slot = outs[ai].at[4 * block[0] + 2 * block[1] + block[2]]
            return pltpu.make_async_remote_copy(
                src_ref=ins[ai] if own else slot, dst_ref=slot, send_sem=send_sems.at[ai, kk],
                recv_sem=recv_sems.at[ai, kk], device_id=to, device_id_type=MESH)

        return (x, y, c), (x, y, 1 - c), chips, c, copy

    def start(ins, outs, sems):
        me, sibling, chips, c, copy = ctx(ins, outs, sems)
        for ai in range(na):
            copy(ai, 0, me, sibling, own=True).start()
            for j, chip in enumerate(chips):
                copy(ai, 1 + j, me, (*chip, c), own=True).start()
        for ai in range(na):
            own_copy(ins, outs, sems, ai).start()

    def mid(ins, outs, sems):
        me, sibling, chips, c, copy = ctx(ins, outs, sems)
        for ai in range(na):
            for j, chip in enumerate(chips):
                copy(ai, 1 + j, (*chip, c), me).wait_recv()
                copy(ai, 4 + j, (*chip, c), sibling).start()

    def end(ins, outs, sems):
        me, sibling, chips, c, copy = ctx(ins, outs, sems)
        for ai in range(na):
            copy(ai, 0, sibling, me).wait_recv()
            copy(ai, 0, me, sibling, own=True).wait_send()
            for j, chip in enumerate(chips):
                copy(ai, 4 + j, (*chip, 1 - c), me).wait_recv()
                copy(ai, 1 + j, me, (*chip, c), own=True).wait_send()
                copy(ai, 4 + j, (*chip, c), sibling).wait_send()
            own_copy(ins, outs, sems, ai).wait()

    return Comm(arrays, [jax.ShapeDtypeStruct((N_DEV,) + a.shape, a.dtype) for a in arrays],
                [pltpu.SemaphoreType.DMA((na, 7)), pltpu.SemaphoreType.DMA((na, 7)), pltpu.SemaphoreType.DMA((na,))],
                start, end, mid)


def _sequencer_gather(arrays, name, collective_id):
    comm = _gather_comm(arrays)
    na = len(arrays)

    def body(*refs):
        ins, outs, sems = refs[:na], refs[na:2 * na], refs[2 * na:]
        x, y, c = _mesh_pos()
        peers = [(x, y, 1 - c), (1 - x, y, c), (x, 1 - y, c), (1 - x, 1 - y, c)]
        barrier = pltpu.get_barrier_semaphore()
        for peer in peers:
            pl.semaphore_signal(barrier, inc=1, device_id=peer, device_id_type=MESH)
        pl.semaphore_wait(barrier, len(peers))
        comm.start(ins, outs, sems)
        comm.mid(ins, outs, sems)
        comm.end(ins, outs, sems)

    return list(pl.kernel(
        body, out_type=tuple(comm.out_shapes), mesh=plsc.ScalarSubcoreMesh(axis_name="sequencer", num_cores=1),
        name=name, scratch_types=tuple(comm.sems),
        compiler_params=pltpu.CompilerParams(collective_id=collective_id))(*arrays))


def _swap_comm(arrays):
    na = len(arrays)
    offs = np.concatenate([[0], np.cumsum([a.shape[1] for a in arrays])]).astype(int)

    def copies(ins, outs, sems):
        x, y, c = _mesh_pos()
        return [pltpu.make_async_remote_copy(
            src_ref=ins[ai].at[2 * k + 1 - c], dst_ref=outs[0].at[k, pl.ds(int(offs[ai]), arrays[ai].shape[1])],
            send_sem=sems[0].at[ai, k], recv_sem=sems[1].at[ai, k], device_id=(x, y, 1 - c), device_id_type=MESH)
            for ai in range(na) for k in range(4)]

    def start(ins, outs, sems):
        for cp in copies(ins, outs, sems):
            cp.start()

    def end(ins, outs, sems):
        for cp in copies(ins, outs, sems):
            cp.wait()

    return Comm(arrays, [jax.ShapeDtypeStruct((4, int(offs[-1]), PACK_COLS), arrays[0].dtype)],
                [pltpu.SemaphoreType.DMA((na, 4)), pltpu.SemaphoreType.DMA((na, 4))], start, end)


def _chips_comm(send):
    def copies(ins, outs, sems):
        x, y, c = _mesh_pos()
        chips = [(1 - x, y), (x, 1 - y), (1 - x, 1 - y)]
        return [pltpu.make_async_remote_copy(
            src_ref=ins[0].at[2 * cx + cy], dst_ref=outs[0].at[j], send_sem=sems[0].at[j], recv_sem=sems[1].at[j],
            device_id=(cx, cy, c), device_id_type=MESH) for j, (cx, cy) in enumerate(chips)]

    def start(ins, outs, sems):
        for cp in copies(ins, outs, sems):
            cp.start()

    def end(ins, outs, sems):
        for cp in copies(ins, outs, sems):
            cp.wait()

    return Comm([send], [jax.ShapeDtypeStruct((3,) + send.shape[1:], send.dtype)],
                [pltpu.SemaphoreType.DMA((3,)), pltpu.SemaphoreType.DMA((3,))], start, end)


def _all_gather(arrays, name):
    na = len(arrays)

    def body(*refs):
        ins, outs = refs[:na], refs[na:2 * na]
        send_sems, recv_sems, local_sems = refs[2 * na:]
        x, y, c = _mesh_pos()
        me, sibling = (x, y, c), (x, y, 1 - c)
        chips = [(1 - x, y), (x, 1 - y), (1 - x, 1 - y)]
        waits = []
        for ai in range(na):
            in_ref, out_ref = ins[ai], outs[ai]

            def slot(px, py, pc, out_ref=out_ref):
                return out_ref.at[4 * px + 2 * py + pc]

            def copy(kk, block, to, src=None, ai=ai, slot=slot):
                return pltpu.make_async_remote_copy(
                    src_ref=slot(*block) if src is None else src, dst_ref=slot(*block),
                    send_sem=send_sems.at[ai, kk], recv_sem=recv_sems.at[ai, kk], device_id=to, device_id_type=MESH)

            mine = pltpu.make_async_copy(in_ref, slot(*me), local_sems.at[ai])
            mine.start()
            first = [copy(0, me, sibling, src=in_ref)]
            first += [copy(1 + j, me, (*chip, c), src=in_ref) for j, chip in enumerate(chips)]
            for cp in first:
                cp.start()
            waits.append((copy, mine, first))
        sends = []
        for ai in range(na):
            copy, mine, first = waits[ai]
            passed = [copy(4 + j, (*chip, c), sibling) for j, chip in enumerate(chips)]
            for j, chip in enumerate(chips):
                copy(1 + j, (*chip, c), me).wait_recv()
                passed[j].start()
            sends.append(passed)
        for ai in range(na):
            copy, mine, first = waits[ai]
            copy(0, sibling, me).wait_recv()
            for j, chip in enumerate(chips):
                copy(4 + j, (*chip, 1 - c), me).wait_recv()
            for cp in first + sends[ai]:
                cp.wait_send()
            mine.wait()

    any_spec = pl.BlockSpec(memory_space=pl.ANY)
    return pl.pallas_call(
        body, name=name, in_specs=[any_spec] * na, out_specs=[any_spec] * na,
        out_shape=[jax.ShapeDtypeStruct((N_DEV,) + a.shape, a.dtype) for a in arrays],
        scratch_shapes=[pltpu.SemaphoreType.DMA((na, 7)), pltpu.SemaphoreType.DMA((na, 7)),
                        pltpu.SemaphoreType.DMA((na,))],
        compiler_params=pltpu.CompilerParams(has_side_effects=True))(*arrays)


def _swap_sibling(arrays, name):
    na = len(arrays)
    offs = np.concatenate([[0], np.cumsum([a.shape[1] for a in arrays])]).astype(int)
    rows = int(offs[-1])

    def body(*refs):
        ins, recv_ref = refs[:na], refs[na]
        send_sems, recv_sems = refs[na + 1:]
        x, y, c = _mesh_pos()
        started = []
        for ai in range(na):
            span = pl.ds(int(offs[ai]), arrays[ai].shape[1])
            for k in range(4):
                remote = pltpu.make_async_remote_copy(
                    src_ref=ins[ai].at[2 * k + 1 - c], dst_ref=recv_ref.at[k, span], send_sem=send_sems.at[ai, k],
                    recv_sem=recv_sems.at[ai, k], device_id=(x, y, 1 - c), device_id_type=MESH)
                remote.start()
                started.append(remote)
        for remote in started:
            remote.wait()

    any_spec = pl.BlockSpec(memory_space=pl.ANY)
    return pl.pallas_call(
        body, name=name, in_specs=[any_spec] * na, out_specs=any_spec,
        out_shape=jax.ShapeDtypeStruct((4, rows, PACK_COLS), arrays[0].dtype),
        scratch_shapes=[pltpu.SemaphoreType.DMA((na, 4)), pltpu.SemaphoreType.DMA((na, 4))])(*arrays)


def _exchange_chips(send, name):
    def body(s_ref, o_ref, send_sems, recv_sems):
        x, y, c = _mesh_pos()
        chips = [(1 - x, y), (x, 1 - y), (1 - x, 1 - y)]
        cps = [pltpu.make_async_remote_copy(
            src_ref=s_ref.at[2 * cx + cy], dst_ref=o_ref.at[j], send_sem=send_sems.at[j], recv_sem=recv_sems.at[j],
            device_id=(cx, cy, c), device_id_type=MESH) for j, (cx, cy) in enumerate(chips)]
        for cp in cps:
            cp.start()
        for cp in cps:
            cp.wait()

    any_spec = pl.BlockSpec(memory_space=pl.ANY)
    return pl.pallas_call(
        body, name=name, in_specs=[any_spec], out_specs=any_spec,
        out_shape=jax.ShapeDtypeStruct((3,) + send.shape[1:], send.dtype),
        scratch_shapes=[pltpu.SemaphoreType.DMA((3,)), pltpu.SemaphoreType.DMA((3,))])(send)


def _pair_sum(keep, recv, name, tr=464):
    nchip, rows, cols = keep.shape

    def body(g_ref, r_ref, o_ref):
        o_ref[...] = (g_ref[...].astype(F32) + r_ref[...].astype(F32)).astype(BF16)

    blk = pl.BlockSpec((1, tr, cols), lambda k, i: (k, i, 0))
    return pl.pallas_call(
        body, name=name, grid=(nchip, rows // tr), in_specs=[blk, blk], out_specs=blk,
        out_shape=jax.ShapeDtypeStruct((nchip, rows, cols), BF16),
        compiler_params=_params(("parallel", "parallel")))(keep, recv)


def _pair_sum_pieces(pieces, recv, name, tr):
    _, rows, cols = pieces.shape
    core = lax.axis_index("c").astype(jnp.int32).reshape(1)

    def body(c_ref, g_ref, r_ref, o_ref):
        del c_ref
        o_ref[...] = (g_ref[...].astype(F32) + r_ref[...].astype(F32)).astype(BF16)

    grid_spec = pltpu.PrefetchScalarGridSpec(
        num_scalar_prefetch=1, grid=(4, rows // tr),
        in_specs=[pl.BlockSpec((1, tr, cols), lambda k, i, c_ref: (2 * k + c_ref[0], i, 0)),
                  pl.BlockSpec((1, tr, cols), lambda k, i, c_ref: (k, i, 0))],
        out_specs=pl.BlockSpec((1, tr, cols), lambda k, i, c_ref: (k, i, 0)))
    return pl.pallas_call(
        body, name=name, grid_spec=grid_spec, out_shape=jax.ShapeDtypeStruct((4, rows, cols), BF16),
        compiler_params=_params(("parallel", "parallel")))(core, pieces, recv)


def _chip_sum(own, others, name, tr=464):
    _, rows, cols = own.shape
    chip = (2 * lax.axis_index("x") + lax.axis_index("y")).astype(jnp.int32).reshape(1)

    def body(chip_ref, own_ref, oth_ref, o_ref):
        del chip_ref
        acc = own_ref[0].astype(F32)
        for j in range(3):
            acc = acc + oth_ref[j].astype(F32)
        o_ref[...] = acc

    grid_spec = pltpu.PrefetchScalarGridSpec(
        num_scalar_prefetch=1, grid=(rows // tr,),
        in_specs=[pl.BlockSpec((1, tr, cols), lambda i, chip_ref: (chip_ref[0], i, 0)),
                  pl.BlockSpec((3, tr, cols), lambda i, chip_ref: (0, i, 0))],
        out_specs=pl.BlockSpec((tr, cols), lambda i, chip_ref: (i, 0)))
    return pl.pallas_call(
        body, name=name, grid_spec=grid_spec, out_shape=jax.ShapeDtypeStruct((rows, cols), F32),
        compiler_params=_params(("parallel",)))(chip, own, others)


def _sum_leading(parts, name, tr=464):
    nparts, rows, cols = parts.shape
    tr = tr if rows % tr == 0 else rows

    def body(p_ref, o_ref):
        acc = p_ref[0].astype(F32)
        for i in range(1, nparts):
            acc = acc + p_ref[i].astype(F32)
        o_ref[...] = acc

    return pl.pallas_call(
        body, name=name, grid=(rows // tr,),
        in_specs=[pl.BlockSpec((nparts, tr, cols), lambda i: (0, i, 0))],
        out_specs=pl.BlockSpec((tr, cols), lambda i: (i, 0)), out_shape=jax.ShapeDtypeStruct((rows, cols), F32),
        compiler_params=_params(("parallel",)))(parts)


def _adamw(w, g, m, v, name, comm=None):
    shape = w.shape
    cols = shape[-1]
    lead = shape[0] if len(shape) >= 3 else 1
    rows = int(np.prod(shape[:-1])) // lead if len(shape) > 1 else 1
    w2, g2, m2, v2 = (a.reshape(lead, rows, cols) for a in (w, g, m, v))
    tr = rows
    for cand in (512, 256, 128, 64, 32, 16, 8):
        if rows % cand == 0 and rows > cand:
            tr = cand
            break
    bc1, bc2 = 1.0 - ADAM_B1 ** ADAM_STEP, 1.0 - ADAM_B2 ** ADAM_STEP

    def body(w_ref, g_ref, m_ref, v_ref, d_ref, nm_ref, nv_ref):
        gv = g_ref[...]
        nm = ADAM_B1 * m_ref[...] + (1.0 - ADAM_B1) * gv
        nv = ADAM_B2 * v_ref[...] + (1.0 - ADAM_B2) * (gv * gv)
        nm_ref[...] = nm
        nv_ref[...] = nv
        d_ref[...] = -ADAM_LR * ((nm / bc1) / (jnp.sqrt(nv / bc2) + ADAM_EPS) + ADAM_WD * w_ref[...])

    blk = pl.BlockSpec((1, tr, cols), lambda l, i: (l, i, 0))
    res = _call(body, name=name, grid=(lead, rows // tr), in_specs=[blk] * 4, out_specs=[blk] * 3,
                out_shape=[jax.ShapeDtypeStruct((lead, rows, cols), F32)] * 3, sem=("parallel", "parallel"),
                args=(w2, g2, m2, v2), comm=comm)
    outs, couts = res if comm is not None else (res, None)
    outs = tuple(o.reshape(shape) for o in outs)
    return outs if comm is None else (outs, couts)


WEIGHT_NAMES = ['norm_mix_g', 'norm_xa_g', 'norm_ffn_g', 'norm_mem_g', 'norm_final_g', 'w_in_ab', 'conv_qkv_a',
                'a_log_a', 'dt_bias_a', 'onorm_g_a', 'ssm_lambda_re', 'ssm_lambda_im', 'ssm_b_re', 'ssm_b_im',
                'ssm_c_re', 'ssm_c_im', 'ssm_d', 'ssm_log_dt', 'w_glu_b', 'b_glu_b', 'w_out_ab', 'pool_w',
                'pool_scale', 'xa_wq', 'xa_wkv', 'xa_wo', 'ffn_w_up', 'ffn_conv', 'ffn_w_down']
BIG_SHARDED = {'w_in_ab': ((1, 1024, 2568), 2), 'w_glu_b': ((1, 512, 512), 1), 'w_out_ab': ((1, 1024, 1024), 1),
               'pool_w': ((1, 4, 256, 256), 2), 'xa_wq': ((2, 1024, 1024), 1), 'xa_wkv': ((2, 1024, 2048), 2),
               'xa_wo': ((2, 1024, 1024), 1), 'ffn_w_up': ((2, 1024, 5632), 2), 'ffn_w_down': ((2, 2816, 1024), 1)}
SMALL_SHARDED = {'conv_qkv_a': ((1, 4, 1536), 2), 'pool_scale': ((1, 1024), 1), 'ffn_conv': ((2, 3, 5632), 2)}
REPLICATED = {'norm_mix_g': (2, 1024), 'norm_xa_g': (2, 1024), 'norm_ffn_g': (2, 1024), 'norm_mem_g': (1024,),
              'norm_final_g': (1024,), 'a_log_a': (1, 4), 'dt_bias_a': (1, 4), 'onorm_g_a': (1, 128),
              'ssm_lambda_re': (1, 32, 64), 'ssm_lambda_im': (1, 32, 64), 'ssm_b_re': (1, 32, 64, 16),
              'ssm_b_im': (1, 32, 64, 16), 'ssm_c_re': (1, 32, 16, 64), 'ssm_c_im': (1, 32, 16, 64),
              'ssm_d': (1, 32, 16), 'ssm_log_dt': (1, 32), 'b_glu_b': (1, 512)}
PACK_ROW_ALIGN = 8


def _shard_shape(shape, axis):
    return tuple(s // N_DEV if i == axis else s for i, s in enumerate(shape))


def _round_up(n, m):
    return (n + m - 1) // m * m


def _pack(arrays):
    total = sum(int(np.prod(a.shape)) for a in arrays)
    padded = _round_up(total, PACK_COLS * PACK_ROW_ALIGN)
    parts = [a.astype(F32).reshape(-1) for a in arrays]
    if padded != total:
        parts.append(jnp.zeros((padded - total,), F32))
    return jnp.concatenate(parts).reshape(padded // PACK_COLS, PACK_COLS)


def _unpack(packed, shapes):
    flat, out, off = packed.reshape(-1), [], 0
    for shape in shapes:
        size = int(np.prod(shape))
        out.append(flat[off:off + size].reshape(shape))
        off += size
    return out


def _split_shards(full, axis):
    shape = full.shape
    s = shape[axis] // N_DEV
    a = full.reshape(shape[:axis] + (N_DEV, s) + shape[axis + 1:])
    return jnp.moveaxis(a, axis, 0).reshape(N_DEV, -1)


def _merge_shards(pieces, shape, axis):
    sh = _shard_shape(shape, axis)
    a = pieces.reshape((N_DEV,) + sh)
    a = jnp.moveaxis(a, 0, axis)
    return a.reshape(shape)


_SCAN_NB = SSM_CH // SCAN_CB


def _to_scan_layout(m, axis):
    shape = m.shape
    m = m.reshape(shape[:axis] + (2, _SCAN_NB, SCAN_CB) + shape[axis + 1:])
    return jnp.swapaxes(m, axis, axis + 1).reshape(shape)


def _from_scan_layout(m, axis):
    shape = m.shape
    m = m.reshape(shape[:axis] + (_SCAN_NB, 2, SCAN_CB) + shape[axis + 1:])
    return jnp.swapaxes(m, axis, axis + 1).reshape(shape)


def _s5_discretise(lam_re, lam_im, b_re, b_im, log_dt):
    dt = jnp.exp(log_dt)[:, None]
    mag = jnp.exp(lam_re * dt)
    ang = lam_im * dt
    lb_re, lb_im = mag * jnp.cos(ang), mag * jnp.sin(ang)
    den = lam_re * lam_re + lam_im * lam_im
    nr, ni = lb_re - 1.0, lb_im
    coef_re = (nr * lam_re + ni * lam_im) / den
    coef_im = (ni * lam_re - nr * lam_im) / den
    bb_re = coef_re[..., None] * b_re - coef_im[..., None] * b_im
    bb_im = coef_re[..., None] * b_im + coef_im[..., None] * b_re
    return lb_re, lb_im, bb_re, bb_im


_GROUPS_PER_BLOCK = N_GROUPS // _SCAN_NB
_U_BLOCK = _GROUPS_PER_BLOCK * SSM_GROUP


def _s5_matrices(lb_re, lb_im, bb_re, bb_im, c_re, c_im):
    eye = jnp.eye(_GROUPS_PER_BLOCK, dtype=F32)
    blocked = lambda m: m.reshape((_SCAN_NB, _GROUPS_PER_BLOCK) + m.shape[1:])
    bmat = lambda bb: jnp.einsum('jgph,gk->jghkp', blocked(bb), eye).reshape(_SCAN_NB, _U_BLOCK, SCAN_CB)
    cmat = lambda cc: jnp.einsum('jghp,gk->jkpgh', blocked(cc), eye).reshape(_SCAN_NB, SCAN_CB, _U_BLOCK)
    b_in = jnp.concatenate([bmat(bb_re), bmat(bb_im)], axis=2)
    c_out = jnp.concatenate([cmat(c_re), -cmat(c_im)], axis=1)
    a_row = _to_scan_layout(jnp.concatenate([lb_re.reshape(1, SSM_CH), lb_im.reshape(1, SSM_CH)], axis=1), 1)
    return b_in, c_out, a_row


def _s5_matrix_grads(db_in, dc_out, da_row):
    da_nat = _from_scan_layout(da_row, 1)
    eye = jnp.eye(_GROUPS_PER_BLOCK, dtype=F32)
    nb, gb = _SCAN_NB, _GROUPS_PER_BLOCK
    bgrad = lambda m: jnp.einsum('jghkp,gk->jgph', m.reshape(nb, gb, SSM_GROUP, gb, SSM_STATE), eye
                                 ).reshape(N_GROUPS, SSM_STATE, SSM_GROUP)
    cgrad = lambda m: jnp.einsum('jkpgh,gk->jghp', m.reshape(nb, gb, SSM_STATE, gb, SSM_GROUP), eye
                                 ).reshape(N_GROUPS, SSM_GROUP, SSM_STATE)
    dbb_re, dbb_im = bgrad(db_in[:, :, :SCAN_CB]), bgrad(db_in[:, :, SCAN_CB:])
    dc_re, dc_im = cgrad(dc_out[:, :SCAN_CB]), -cgrad(dc_out[:, SCAN_CB:])
    dlb_re = da_nat[0, :SSM_CH].reshape(N_GROUPS, SSM_STATE)
    dlb_im = da_nat[0, SSM_CH:].reshape(N_GROUPS, SSM_STATE)
    return dlb_re, dlb_im, dbb_re, dbb_im, dc_re, dc_im


def _as_pieces(a):
    return a.reshape(N_DEV, a.shape[0] // N_DEV, a.shape[1])


def _hybrid_fwd(xn, x, wts, p, weights, riders):
    sv = {}
    hq = _mm(xn, wts['w_qkv_t'], "nt", "l0_in_qkv")
    gate = _mm(xn, wts['w_gate_t'], "nt", "l0_in_gate")
    ba = _mm(xn, wts['w_ba_t'], "nt", "l0_in_ba")
    u = _mm(xn, wts['w_u_t'], "nt", "l0_in_u")
    conv = p['conv_qkv']
    q = _qkv_pre_fwd(hq, conv, 0, 4, True, HEAD_A ** -0.5, "l0_q_pre")
    k = _qkv_pre_fwd(hq, conv, 4, 4, True, 1.0, "l0_k_pre")
    v = _qkv_pre_fwd(hq, conv, 8, 4, False, 1.0, "l0_v_pre")
    gates = _gates_fwd(ba, p['arow'], p['brow'], "l0_gates")
    o, tm_all, s_all = riders.run("l0_gdr_fwd", _gdr_fwd, q, k, v, gates)
    wts['w_glu'], wts['w_out'] = weights.full['w_glu'], weights.full['w_out']
    y_a = _onorm_fwd(o, gate, p['onorm_g'], "l0_onorm")
    bu = riders.run("l0_s5_bu", _mm_bd, u, p['b_in'], "nn")
    xs = riders.run("l0_s5_scan", _s5_scan_fwd, bu, p['a_row'])
    weights.gather_by_sequencer(GATHER_LAYER1, xs, "gather_layer1", GATHER_LAYER1_ID)
    yc = riders.run("l0_s5_cx", _mm_bd, xs, p['c_out'], "nn")
    yl, y_b = _glu_fwd(yc, u, p['d_row'], wts['w_glu'], p['b_glu'], "l0_glu")
    mixed = jnp.concatenate([y_a, y_b], axis=1)
    x1 = _mm(mixed, wts['w_out'], "nn", "l0_out", res=x)
    sv.update(hq=hq, gate=gate, ba=ba, u=u, q=q, k=k, v=v, gb=gates, o=o, tm=tm_all, s=s_all, xs=xs, yl=yl, mixed=mixed)
    return x1, sv


def _hybrid_bwd(dx1, xn, wts, p, sv, riders):
    gr = {}
    dmixed = _mm(dx1, wts['w_out'], "nt", "l0_out_dx", out_dtype=BF16)
    riders.grad('w_out', _as_pieces(_mm(sv['mixed'], dx1, "tn", "l0_out_dw", out_dtype=BF16)))
    dya, dyb = dmixed[:, :WIDTH_A], dmixed[:, WIDTH_A:]
    dyl, du_direct, dw_glu, gr['b_glu_b'], dd = _glu_bwd(
        sv['yl'], sv['u'], p['d_row'], wts['w_glu'], p['b_glu'], dyb, "l0_glu_bwd")
    riders.grad('w_glu', dw_glu.astype(BF16).reshape(N_DEV, -1, PACK_COLS))
    dxs = riders.run("l0_s5_cx_dx", _mm_bd, dyl, p['c_out'], "nt")
    dc_out = _mm_bd(sv['xs'], dyl, "tn", "l0_s5_cx_dw")
    lam, da_row = riders.run("l0_s5_scan_bwd", _s5_scan_bwd, dxs, sv['xs'], p['a_row'])
    du = _mm_bd(lam, p['b_in'], "nt", "l0_s5_bu_dx", res=du_direct, out_dtype=BF16)
    db_in = _mm_bd(sv['u'], lam, "tn", "l0_s5_bu_dw")
    gr['s5'] = (db_in, dc_out, da_row, dd)
    do, dgate, gr['onorm_g_a'] = _onorm_bwd(sv['o'], sv['gate'], p['onorm_g'], dya, "l0_onorm_bwd")
    dq, dk, dv, dgb = riders.run("l0_gdr_bwd", _gdr_bwd, sv['q'], sv['k'], sv['v'], sv['gb'], sv['tm'], sv['s'], do)
    conv = p['conv_qkv']
    dhq_q, dcw_q = _qkv_pre_bwd(sv['hq'], conv, dq, 0, 4, True, HEAD_A ** -0.5, "l0_q_pre_bwd")
    dhq_k, dcw_k = _qkv_pre_bwd(sv['hq'], conv, dk, 4, 4, True, 1.0, "l0_k_pre_bwd")
    dhq_v, dcw_v = _qkv_pre_bwd(sv['hq'], conv, dv, 8, 4, False, 1.0, "l0_v_pre_bwd")
    gr['conv_qkv_a'] = jnp.concatenate([dcw_q, dcw_k, dcw_v], axis=1)
    dhq = jnp.concatenate([dhq_q, dhq_k, dhq_v], axis=1)
    dba, da_log, ddt_bias = _gates_bwd(sv['ba'], p['arow'], p['brow'], dgb, "l0_gates_bwd")
    gr['a_log_a'], gr['dt_bias_a'] = da_log[:, 4:8], ddt_bias[:, 4:8]
    dw_qkv_t = _mm(dhq, xn, "tn", "l0_in_qkv_dw", out_dtype=BF16)
    dw_gate_t = _mm(dgate, xn, "tn", "l0_in_gate_dw", out_dtype=BF16)
    dw_ba_t = _mm(dba, xn, "tn", "l0_in_ba_dw", out_dtype=BF16)
    dw_u_t = _mm(du, xn, "tn", "l0_in_u_dw", out_dtype=BF16)
    dw_in_t = _as_pieces(jnp.concatenate([dw_qkv_t, dw_gate_t, dw_ba_t[:8], dw_u_t], axis=0))
    riders.grad('w_in_t', jnp.concatenate(
        [dw_in_t, jnp.zeros((N_DEV, dict(PIECES)['w_in_t'] - W_IN_PIECE, D_MODEL), BF16)], axis=1))
    dxn = riders.run("l0_in_qkv_dx", _mm, dhq, wts['w_qkv_t'], "nn")
    dxn = _mm(dgate, wts['w_gate_t'], "nn", "l0_in_gate_dx", res=dxn)
    dxn = _mm(dba, wts['w_ba_t'], "nn", "l0_in_ba_dx", res=dxn)
    dxn = _mm(du, wts['w_u_t'], "nn", "l0_in_u_dx", res=dxn)
    return dxn, gr


def _xa_fwd(x1, g, mem_n, wq, wkv_t, wo, tag, riders):
    xq = _rms_fwd(x1, g, BF16, tag + "_norm")
    q = _mm(xq, wq, "nn", tag + "_q", out_dtype=BF16)
    kv = _mm(mem_n, wkv_t, "nt", tag + "_kv", out_dtype=BF16)
    o = riders.run(tag + "_attn", _attn_fwd, q, kv)
    x2 = _mm(o, wo, "nn", tag + "_o", res=x1)
    return x2, dict(xq=xq, q=q, kv=kv, o=o)


def _xa_bwd(dx2, x1, g, mem_n, wq, wkv_t, wo, sv, tag, layer, riders):
    do = _mm(dx2, wo, "nt", tag + "_o_dx", out_dtype=BF16)
    riders.grad('wo%d' % layer, _as_pieces(_mm(sv['o'], dx2, "tn", tag + "_o_dw", out_dtype=BF16)))
    dq, dk, dv = _attn_bwd(sv['q'], sv['kv'], do, tag + "_attn_bwd")
    dkv = jnp.concatenate([dk, dv], axis=1).astype(BF16)
    dxq = _mm(dq, wq, "nt", tag + "_q_dx")
    riders.grad('wq%d' % layer, _as_pieces(_mm(sv['xq'], dq, "tn", tag + "_q_dw", out_dtype=BF16)))
    dmem_n = _mm(dkv, wkv_t, "nn", tag + "_kv_dx")
    riders.grad('wkv_t%d' % layer, _as_pieces(_mm(dkv, mem_n, "tn", tag + "_kv_dw", out_dtype=BF16)))
    dx1, dg = riders.run(tag + "_norm_bwd", _rms_bwd, x1, g, dxq, dx2)
    return dx1, dmem_n, dg


def _ffn_fwd(x2, g, w_up_t, conv, w_down, tag, riders):
    xf = _rms_fwd(x2, g, BF16, tag + "_norm")
    h = riders.run(tag + "_up", _mm, xf, w_up_t, "nt")
    a = riders.run(tag + "_act", _ffn_act_fwd, h, conv)
    x3 = _mm(a, w_down, "nn", tag + "_down", res=x2)
    return x3, dict(xf=xf, h=h, a=a)


def _ffn_bwd(dx3, x2, g, w_up_t, conv, w_down, sv, tag, layer, riders):
    da = _mm(dx3, w_down, "nt", tag + "_down_dx")
    riders.grad('down%d' % layer, _as_pieces(_mm(sv['a'], dx3, "tn", tag + "_down_dw", out_dtype=BF16)))
    dh, dconv = riders.run(tag + "_act_bwd", _ffn_act_bwd, sv['h'], conv, da)
    dxf = riders.run(tag + "_up_dx", _mm, dh, w_up_t, "nn")
    dw_up_t = riders.run(tag + "_up_dw", _mm, dh, sv['xf'], "tn", out_dtype=BF16)
    riders.grad('up_t%d' % layer, _as_pieces(dw_up_t))
    dx2, dg = riders.run(tag + "_norm_bwd", _rms_bwd, x2, g, dxf, dx3)
    return dx2, dconv, dg


BIG_NAMES, SMALL_NAMES, REP_NAMES = list(BIG_SHARDED), list(SMALL_SHARDED), list(REPLICATED)
BIG_SIZES = [int(np.prod(_shard_shape(*BIG_SHARDED[n]))) for n in BIG_NAMES]
SMALL_SIZES = [int(np.prod(_shard_shape(*SMALL_SHARDED[n]))) for n in SMALL_NAMES]


PIECES = [('w_in_t', 384), ('w_glu', 32), ('w_out', 128), ('pool_w', 32), ('wq0', 128), ('wq1', 128),
          ('wkv_t0', 256), ('wkv_t1', 256), ('wo0', 128), ('wo1', 128), ('up_t0', 704), ('up_t1', 704),
          ('down0', 352), ('down1', 352)]
PIECE_OFFS = dict(zip([k for k, _ in PIECES], np.concatenate([[0], np.cumsum([r for _, r in PIECES])[:-1]]).tolist()))
W_IN_ROWS = 4 * WIDTH_A + 2 * N_HEADS_A + SSM_WIDTH
W_IN_PIECE = W_IN_ROWS // N_DEV


def _row_tile(rows):
    return max(t for t in range(16, min(rows, 512) + 1, 16) if rows % t == 0)


class _Riders:
    def __init__(self):
        self.waiting = {}
        self.grads = {}
        self.groups = []
        self.reduced = {}

    def add(self, host, comm, then):
        self.waiting.setdefault(host, []).append((comm, then))

    def run(self, name, fn, *args, **kw):
        riders = self.waiting.pop(name, [])
        if not riders:
            return fn(*args, name=name, **kw)
        out, couts = fn(*args, name=name, comm=[c for c, _ in riders], **kw)
        for (_, then), got in zip(riders, couts):
            then(got)
        return out

    def exchange(self, comm, host, name, then):
        if host is None:
            then(_comm_only(comm, name))
        else:
            self.add(host, comm, then)

    def grad(self, key, pieces):
        self.grads[key] = pieces
        for group in [g for g in self.groups if all(k in self.grads for k in g[1])]:
            self.groups.remove(group)
            self._reduce(*group)

    def _reduce(self, name, keys, swap_host, chips_host):
        arrays = [self.grads[k] for k in keys]
        rows = sum(a.shape[1] for a in arrays)
        tile = _row_tile(rows)

        def after_chips(chip_sums, got):
            total = _chip_sum(chip_sums, got[0], name + "_chip_sum", tr=tile)
            off = 0
            for k, a in zip(keys, arrays):
                self.reduced[k] = total[off:off + a.shape[1]]
                off += a.shape[1]

        def after_swap(got):
            if len(arrays) == 1:
                chip_sums = _pair_sum_pieces(arrays[0], got[0], name + "_pair_sum", tr=tile)
            else:
                core = lax.axis_index("c")
                keep = jnp.concatenate(
                    [lax.dynamic_index_in_dim(a.reshape(4, 2, a.shape[1], PACK_COLS), core, 1, keepdims=False)
                     for a in arrays], axis=1)
                chip_sums = _pair_sum(keep, got[0], name + "_pair_sum", tr=tile)
            self.exchange(_chips_comm(chip_sums), chips_host, name + "_to_chips",
                          functools.partial(after_chips, chip_sums))

        self.exchange(_swap_comm(arrays), swap_host, name + "_to_sibling", after_swap)


class _Weights:
    def __init__(self, inp):
        bf = lambda a: a.astype(BF16)
        local = {'w_in_t': bf(inp['w_in_ab'][0]).T, 'w_glu': bf(inp['w_glu_b'][0]), 'w_out': bf(inp['w_out_ab'][0]),
                 'pool_w': bf(inp['pool_w'][0]),
                 'small': _pack([inp[n] for n in SMALL_NAMES])}
        for l in range(2):
            local['wq%d' % l] = bf(inp['xa_wq'][l])
            local['wkv_t%d' % l] = bf(inp['xa_wkv'][l]).T
            local['wo%d' % l] = bf(inp['xa_wo'][l])
            local['up_t%d' % l] = bf(inp['ffn_w_up'][l]).T
            local['down%d' % l] = bf(inp['ffn_w_down'][l])
        self.local, self.full = local, {}

    def plan(self, keys):
        return _gather_comm([self.local[k] for k in keys])

    def gather_by_sequencer(self, keys, after, name, collective_id):
        arrays = [self.local[k] for k in keys]
        tie = (after.reshape(-1)[0] * 0.0).astype(arrays[0].dtype)
        arrays[0] = arrays[0] + tie
        self.land(keys, _sequencer_gather(arrays, name, collective_id))

    def land(self, keys, gathered):
        for k, g in zip(keys, gathered):
            if k == 'small':
                off = 0
                for n, size in zip(SMALL_NAMES, SMALL_SIZES):
                    self.full[n] = _merge_shards(g.reshape(N_DEV, -1)[:, off:off + size], *SMALL_SHARDED[n])
                    off += size
            elif k == 'pool_w':
                self.full[k] = jnp.swapaxes(g, 0, 1).reshape(len(POOL_WINDOWS), POOL_GROUP, POOL_GROUP)
            else:
                self.full[k] = g.reshape(N_DEV * g.shape[1], g.shape[2])


GATHER_FIRST = ['w_in_t', 'small']
GATHER_RIDES = []
GATHER_LAYER0 = ['w_glu', 'w_out', 'wq0', 'wkv_t0', 'wo0', 'down0', 'up_t0']
GATHER_LAYER1 = ['pool_w', 'wq1', 'wkv_t1', 'wo1', 'up_t1', 'down1']
GATHER_LAYER0_ID, GATHER_LAYER1_ID = 7, 8
GRAD_RIDES = [('g_down1', ['down1'], 'l1_ffn_act_bwd', 'l1_ffn_up_dx'),
              ('g_up1', ['up_t1'], 'l1_ffn_norm_bwd', 'l0_ffn_act_bwd'),
              ('g_xa1', ['wq1', 'wkv_t1', 'wo1', 'pool_w'], 'l1_mix_norm_bwd', 'l0_ffn_up_dx'),
              ('g_down0', ['down0'], 'l0_ffn_act_bwd', 'l0_ffn_up_dw'),
              ('g_l0', ['up_t0', 'wq0', 'wkv_t0', 'wo0'], 'l0_xa_norm_bwd', 'l0_gdr_bwd'),
              ('g_out', ['w_out', 'w_glu'], 'l0_s5_cx_dx', 'l0_s5_scan_bwd'),
              ('g_in', ['w_in_t'], 'l0_in_qkv_dx', 'adamw_ffn_w_down')]


def _local_step(inp):
    f32_of = lambda n: inp[n].astype(F32)
    weights = _Weights(inp)
    riders = _Riders()
    riders.groups = list(GRAD_RIDES)
    full = weights.full
    weights.land(GATHER_FIRST, _comm_only(weights.plan(GATHER_FIRST), "gather_first"))
    weights.gather_by_sequencer(GATHER_LAYER0, full['w_in_t'], "gather_layer0", GATHER_LAYER0_ID)
    for host, keys in GATHER_RIDES:
        riders.add(host, weights.plan(keys), functools.partial(weights.land, keys))
    w_in_t = full['w_in_t']
    wts0 = dict(w_qkv_t=w_in_t[:3 * WIDTH_A], w_gate_t=w_in_t[3 * WIDTH_A:4 * WIDTH_A],
                w_ba_t=jnp.concatenate([w_in_t[4 * WIDTH_A:4 * WIDTH_A + 8], jnp.zeros((LANE - 8, D_MODEL), BF16)], 0),
                w_u_t=w_in_t[4 * WIDTH_A + 8:])
    lb_disc, disc_vjp = jax.vjp(_s5_discretise, f32_of('ssm_lambda_re')[0], f32_of('ssm_lambda_im')[0],
                                f32_of('ssm_b_re')[0], f32_of('ssm_b_im')[0], f32_of('ssm_log_dt')[0])
    b_in, c_out, a_row = _s5_matrices(*lb_disc, f32_of('ssm_c_re')[0], f32_of('ssm_c_im')[0])
    zeros4 = jnp.zeros((1, 4), F32)
    p0 = dict(conv_qkv=full['conv_qkv_a'][0], onorm_g=f32_of('onorm_g_a'),
              arow=jnp.concatenate([zeros4, f32_of('a_log_a'), jnp.zeros((1, LANE - 8), F32)], 1),
              brow=jnp.concatenate([zeros4, f32_of('dt_bias_a'), jnp.zeros((1, LANE - 8), F32)], 1),
              b_in=b_in.astype(BF16), c_out=c_out.astype(BF16), a_row=a_row,
              d_row=f32_of('ssm_d').reshape(1, SSM_WIDTH), b_glu=f32_of('b_glu_b'))

    x0 = inp['x'][0]
    mem_n = _rms_fwd(inp['mem'][0], inp['norm_mem_g'], BF16, "mem_norm")
    xn0 = _rms_fwd(x0, inp['norm_mix_g'][0], BF16, "l0_mix_norm")
    x1, sv_mix0 = _hybrid_fwd(xn0, x0, wts0, p0, weights, riders)
    x2, sv_xa0 = _xa_fwd(x1, inp['norm_xa_g'][0], mem_n, full['wq0'], full['wkv_t0'], full['wo0'], "l0_xa", riders)
    x3, sv_ffn0 = _ffn_fwd(x2, inp['norm_ffn_g'][0], full['up_t0'], full['ffn_conv'][0], full['down0'], "l0_ffn", riders)
    xn1 = _rms_fwd(x3, inp['norm_mix_g'][1], F32, "l1_mix_norm")
    x4 = _pool_fwd(xn1, full['pool_w'], full['pool_scale'], x3, "l1_pool")
    x5, sv_xa1 = _xa_fwd(x4, inp['norm_xa_g'][1], mem_n, full['wq1'], full['wkv_t1'], full['wo1'], "l1_xa", riders)
    x6, sv_ffn1 = _ffn_fwd(x5, inp['norm_ffn_g'][1], full['up_t1'], full['ffn_conv'][1], full['down1'], "l1_ffn", riders)
    loss_part, dx6, dg_final = _loss_head(x6, inp['norm_final_g'], inp['loss_target'][0], "loss_head")

    dx5, dconv1, dg_ffn1 = _ffn_bwd(dx6, x5, inp['norm_ffn_g'][1], full['up_t1'], full['ffn_conv'][1], full['down1'],
                                    sv_ffn1, "l1_ffn", 1, riders)
    dx4, dmem1, dg_xa1 = _xa_bwd(dx5, x4, inp['norm_xa_g'][1], mem_n, full['wq1'], full['wkv_t1'], full['wo1'],
                                 sv_xa1, "l1_xa", 1, riders)
    dxn1, dpool_w, dpool_scale = _pool_bwd(xn1, full['pool_w'], full['pool_scale'], dx4, "l1_pool_bwd")
    pool_pieces = jnp.swapaxes(dpool_w.astype(BF16).reshape(len(POOL_WINDOWS), N_DEV, -1, POOL_GROUP), 0, 1)
    riders.grad('pool_w', pool_pieces.reshape(N_DEV, -1, PACK_COLS))
    dx3, dg_mix1 = riders.run("l1_mix_norm_bwd", _rms_bwd, x3, inp['norm_mix_g'][1], dxn1, dx4)
    dx2, dconv0, dg_ffn0 = _ffn_bwd(dx3, x2, inp['norm_ffn_g'][0], full['up_t0'], full['ffn_conv'][0], full['down0'],
                                    sv_ffn0, "l0_ffn", 0, riders)
    dx1, dmem0, dg_xa0 = _xa_bwd(dx2, x1, inp['norm_xa_g'][0], mem_n, full['wq0'], full['wkv_t0'], full['wo0'],
                                 sv_xa0, "l0_xa", 0, riders)
    dxn0, g_mix0 = _hybrid_bwd(dx1, xn0, wts0, p0, sv_mix0, riders)
    grad_x, dg_mix0 = _rms_bwd(x0, inp['norm_mix_g'][0], dxn0, dx1, "l0_mix_norm_bwd")
    _, dg_mem = _rms_bwd(inp['mem'][0], inp['norm_mem_g'], dmem0 + dmem1, None, "mem_norm_bwd")
    assert not riders.groups and all(k.startswith("adamw_") for k in riders.waiting), (list(riders.waiting), riders.groups)

    db_in, dc_out, da_row, dd = g_mix0['s5']
    dlb_re, dlb_im, dbb_re, dbb_im, dc_re, dc_im = _s5_matrix_grads(db_in, dc_out, da_row)
    dlam_re, dlam_im, dbr, dbi, dlog_dt = disc_vjp((dlb_re, dlb_im, dbb_re, dbb_im))

    rep_grads = {
        'norm_mix_g': jnp.concatenate([dg_mix0, dg_mix1], 0), 'norm_xa_g': jnp.concatenate([dg_xa0, dg_xa1], 0),
        'norm_ffn_g': jnp.concatenate([dg_ffn0, dg_ffn1], 0), 'norm_mem_g': dg_mem.reshape(-1),
        'norm_final_g': dg_final.reshape(-1), 'a_log_a': g_mix0['a_log_a'], 'dt_bias_a': g_mix0['dt_bias_a'],
        'onorm_g_a': g_mix0['onorm_g_a'], 'ssm_lambda_re': dlam_re[None], 'ssm_lambda_im': dlam_im[None],
        'ssm_b_re': dbr[None], 'ssm_b_im': dbi[None], 'ssm_c_re': dc_re[None], 'ssm_c_im': dc_im[None],
        'ssm_d': dd.reshape(1, N_GROUPS, SSM_GROUP), 'ssm_log_dt': dlog_dt[None], 'b_glu_b': g_mix0['b_glu_b']}
    small_grads = {'conv_qkv_a': g_mix0['conv_qkv_a'][None], 'pool_scale': dpool_scale,
                   'ffn_conv': jnp.stack([dconv0, dconv1])}
    return loss_part, grad_x, riders, rep_grads, small_grads


ADAMW_ORDER = ['ffn_w_up', 'ffn_w_down', 'xa_wkv', 'xa_wq', 'xa_wo', 'w_out_ab', 'w_glu_b', 'pool_w', 'w_in_ab']
SMALL_GRADS_RIDE_ON = "adamw_ffn_w_up"


def _update(inp, loss_part, grad_x, riders, rep_grads, small_grads):
    dev = _device_index()
    gathered = {}
    misc_local = _pack([rep_grads[n] for n in REP_NAMES] + [small_grads[n] for n in SMALL_NAMES] + [loss_part])
    riders.add(SMALL_GRADS_RIDE_ON, _gather_comm([misc_local]), lambda got: gathered.update(misc=got[0]))
    piece = lambda key: riders.reduced[key]
    both = lambda name: jnp.stack([piece(name + '0'), piece(name + '1')])
    swap = lambda a: jnp.swapaxes(a, -1, -2)
    reduced = {'w_in_ab': lambda: piece('w_in_t')[:W_IN_PIECE][None],
               'w_glu_b': lambda: piece('w_glu').reshape(inp['w_glu_b'].shape),
               'w_out_ab': lambda: piece('w_out')[None], 'pool_w': lambda: piece('pool_w').reshape(inp['pool_w'].shape),
               'xa_wq': lambda: both('wq'), 'xa_wkv': lambda: both('wkv_t'), 'xa_wo': lambda: both('wo'),
               'ffn_w_up': lambda: both('up_t'), 'ffn_w_down': lambda: both('down')}
    transposed = ('w_in_ab', 'xa_wkv', 'ffn_w_up')
    grads, upd = {}, {}
    assert sorted(ADAMW_ORDER) == sorted(BIG_NAMES)
    for n in ADAMW_ORDER:
        fix = swap if n in transposed else (lambda a: a)
        g = reduced[n]()
        out = riders.run("adamw_" + n, _adamw, fix(inp[n]), g, fix(inp['m_' + n]), fix(inp['v_' + n]))
        upd[n], grads[n] = tuple(fix(o) for o in out), fix(g)
    assert not riders.waiting, list(riders.waiting)
    misc_sum = _sum_leading(gathered['misc'], "small_grads_sum")
    misc = _unpack(misc_sum, [inp[n].shape for n in REP_NAMES] + [SMALL_SHARDED[n][0] for n in SMALL_NAMES] + [()])
    loss = misc.pop()
    for n, g in zip(REP_NAMES, misc):
        grads[n] = g
    for n, g in zip(SMALL_NAMES, misc[len(REP_NAMES):]):
        grads[n] = lax.dynamic_index_in_dim(_split_shards(g, SMALL_SHARDED[n][1]), dev, 0, keepdims=False
                                            ).reshape(inp[n].shape)
    tiny_names = REP_NAMES + SMALL_NAMES
    rep_total = sum(int(np.prod(inp[n].shape)) for n in REP_NAMES)
    packs = [_pack([inp[prefix + n] for n in tiny_names]) for prefix in ('', 'm_', 'v_')]
    g_pack = _pack([misc_sum.reshape(-1)[:rep_total]] + [grads[n] for n in SMALL_NAMES])
    tiny_out = [_unpack(o, [inp[n].shape for n in tiny_names])
                for o in _adamw(packs[0], g_pack, packs[1], packs[2], "adamw_small")]
    for i, n in enumerate(tiny_names):
        upd[n] = tuple(o[i] for o in tiny_out)

    outs = [loss, grad_x[None]]
    outs += [grads[n] for n in WEIGHT_NAMES]
    for i in range(3):
        outs += [upd[n][i] for n in WEIGHT_NAMES]
    return tuple(outs)


def _step(inp):
    loss_part, grad_x, riders, rep_grads, small_grads = _local_step(inp)
    return _update(inp, loss_part, grad_x, riders, rep_grads, small_grads)


INPUT_NAMES = (['x', 'mem'] + WEIGHT_NAMES + ['loss_target'] + ['m_' + n for n in WEIGHT_NAMES]
               + ['v_' + n for n in WEIGHT_NAMES])


def kernel(x, mem, norm_mix_g, norm_xa_g, norm_ffn_g, norm_mem_g, norm_final_g, w_in_ab, conv_qkv_a, a_log_a, dt_bias_a, onorm_g_a, ssm_lambda_re, ssm_lambda_im, ssm_b_re, ssm_b_im, ssm_c_re, ssm_c_im, ssm_d, ssm_log_dt, w_glu_b, b_glu_b, w_out_ab, pool_w, pool_scale, xa_wq, xa_wkv, xa_wo, ffn_w_up, ffn_conv, ffn_w_down, loss_target, m_norm_mix_g, m_norm_xa_g, m_norm_ffn_g, m_norm_mem_g, m_norm_final_g, m_w_in_ab, m_conv_qkv_a, m_a_log_a, m_dt_bias_a, m_onorm_g_a, m_ssm_lambda_re, m_ssm_lambda_im, m_ssm_b_re, m_ssm_b_im, m_ssm_c_re, m_ssm_c_im, m_ssm_d, m_ssm_log_dt, m_w_glu_b, m_b_glu_b, m_w_out_ab, m_pool_w, m_pool_scale, m_xa_wq, m_xa_wkv, m_xa_wo, m_ffn_w_up, m_ffn_conv, m_ffn_w_down, v_norm_mix_g, v_norm_xa_g, v_norm_ffn_g, v_norm_mem_g, v_norm_final_g, v_w_in_ab, v_conv_qkv_a, v_a_log_a, v_dt_bias_a, v_onorm_g_a, v_ssm_lambda_re, v_ssm_lambda_im, v_ssm_b_re, v_ssm_b_im, v_ssm_c_re, v_ssm_c_im, v_ssm_d, v_ssm_log_dt, v_w_glu_b, v_b_glu_b, v_w_out_ab, v_pool_w, v_pool_scale, v_xa_wq, v_xa_wkv, v_xa_wo, v_ffn_w_up, v_ffn_conv, v_ffn_w_down):
    args = (x, mem, norm_mix_g, norm_xa_g, norm_ffn_g, norm_mem_g, norm_final_g, w_in_ab, conv_qkv_a, a_log_a, dt_bias_a, onorm_g_a, ssm_lambda_re, ssm_lambda_im, ssm_b_re, ssm_b_im, ssm_c_re, ssm_c_im, ssm_d, ssm_log_dt, w_glu_b, b_glu_b, w_out_ab, pool_w, pool_scale, xa_wq, xa_wkv, xa_wo, ffn_w_up, ffn_conv, ffn_w_down, loss_target, m_norm_mix_g, m_norm_xa_g, m_norm_ffn_g, m_norm_mem_g, m_norm_final_g, m_w_in_ab, m_conv_qkv_a, m_a_log_a, m_dt_bias_a, m_onorm_g_a, m_ssm_lambda_re, m_ssm_lambda_im, m_ssm_b_re, m_ssm_b_im, m_ssm_c_re, m_ssm_c_im, m_ssm_d, m_ssm_log_dt, m_w_glu_b, m_b_glu_b, m_w_out_ab, m_pool_w, m_pool_scale, m_xa_wq, m_xa_wkv, m_xa_wo, m_ffn_w_up, m_ffn_conv, m_ffn_w_down, v_norm_mix_g, v_norm_xa_g, v_norm_ffn_g, v_norm_mem_g, v_norm_final_g, v_w_in_ab, v_conv_qkv_a, v_a_log_a, v_dt_bias_a, v_onorm_g_a, v_ssm_lambda_re, v_ssm_lambda_im, v_ssm_b_re, v_ssm_b_im, v_ssm_c_re, v_ssm_c_im, v_ssm_d, v_ssm_log_dt, v_w_glu_b, v_b_glu_b, v_w_out_ab, v_pool_w, v_pool_scale, v_xa_wq, v_xa_wkv, v_xa_wo, v_ffn_w_up, v_ffn_conv, v_ffn_w_down)
    return _step(dict(zip(INPUT_NAMES, args)))
```

```python
import functools
import math

import numpy as np
import jax
import jax.numpy as jnp
from jax import lax
from jax.experimental import pallas as pl
from jax.experimental.pallas import tpu as pltpu
from jax.experimental.pallas import tpu_sc as plsc

F32, BF16 = jnp.float32, jnp.bfloat16
HIGH, HIGHEST = lax.Precision.HIGH, lax.Precision.HIGHEST
MESH = pl.DeviceIdType.MESH

N_DEV = 8
SEQ, D_MODEL, MEM_LEN = 2048, 1024, 256
WIDTH_A, N_HEADS_A, HEAD_A, CONV_A = 512, 4, 128, 4
GDR_CHUNK = 128
GDR_HEADS_PER_STEP = 4
SSM_WIDTH, SSM_GROUP, N_GROUPS, SSM_STATE = 512, 16, 32, 64
SSM_CH = N_GROUPS * SSM_STATE
SCAN_CB = 512
POOL_WINDOWS = (2, 4, 8, 16)
POOL_GROUP = 256
N_HEADS_X, HEAD_X = 4, 256
D_FF, CONV_FFN = 2816, 3
RMS_EPS = 1e-6
ADAM_LR, ADAM_B1, ADAM_B2, ADAM_EPS, ADAM_WD, ADAM_STEP = 0.001, 0.9, 0.999, 1e-08, 0.01, 10
LANE = 128
PACK_COLS = 1024
VMEM_LIMIT_BYTES = 56 * 1024 * 1024


def _params(sem=None):
    return pltpu.CompilerParams(dimension_semantics=sem, vmem_limit_bytes=VMEM_LIMIT_BYTES)


class Comm:
    def __init__(self, inputs, out_shapes, sems, start, end, mid=None):
        self.inputs, self.out_shapes, self.sems = list(inputs), list(out_shapes), list(sems)
        self.start, self.mid, self.end = start, mid, end


def _merge_comms(comms):
    comms = [c for c in comms if c is not None]
    if not comms:
        return None, []
    bounds, ni, no, ns = [], 0, 0, 0
    for c in comms:
        bounds.append((ni, no, ns))
        ni, no, ns = ni + len(c.inputs), no + len(c.out_shapes), ns + len(c.sems)

    def phase(which):
        def run(ins, outs, sems):
            for c, (i0, o0, s0) in zip(comms, bounds):
                fn = getattr(c, which)
                if fn is not None:
                    fn(ins[i0:i0 + len(c.inputs)], outs[o0:o0 + len(c.out_shapes)], sems[s0:s0 + len(c.sems)])
        return run

    merged = Comm([a for c in comms for a in c.inputs], [s for c in comms for s in c.out_shapes],
                  [s for c in comms for s in c.sems], phase("start"), phase("end"), phase("mid"))
    return merged, [(o0, o0 + len(c.out_shapes)) for c, (_, o0, _) in zip(comms, bounds)]


def _call(body, *, name, grid, in_specs, out_specs, out_shape, args, scratch_shapes=(), sem=None, comm=None):
    single = not isinstance(out_shape, (list, tuple))
    out_specs_l = [out_specs] if single else list(out_specs)
    out_shape_l = [out_shape] if single else list(out_shape)
    scratch_shapes = list(scratch_shapes)
    merged, spans = _merge_comms(comm if isinstance(comm, (list, tuple)) else [comm])
    if merged is None:
        outs = pl.pallas_call(body, name=name, grid=grid, in_specs=list(in_specs), out_specs=out_specs_l,
                              out_shape=out_shape_l, scratch_shapes=scratch_shapes, compiler_params=_params(sem))(*args)
        outs = outs[0] if single else outs
        return outs if comm is None else (outs, [])
    n_in, n_out, n_scr = len(in_specs), len(out_specs_l), len(scratch_shapes)
    ci, co = len(merged.inputs), len(merged.out_shapes)
    total = int(np.prod(grid))

    def wrapped(*refs):
        ins, cins = refs[:n_in], refs[n_in:n_in + ci]
        outs, couts = refs[n_in + ci:n_in + ci + n_out], refs[n_in + ci + n_out:n_in + ci + n_out + co]
        scr, csems = refs[n_in + ci + n_out + co:n_in + ci + n_out + co + n_scr], refs[n_in + ci + n_out + co + n_scr:]
        lin = pl.program_id(0)
        for d in range(1, len(grid)):
            lin = lin * grid[d] + pl.program_id(d)
        pl.when(lin == 0)(lambda: merged.start(cins, couts, csems))
        body(*ins, *outs, *scr)
        mid_step = min((3 * total) // 4, total - 1)
        pl.when(lin == mid_step)(lambda: merged.mid(cins, couts, csems))
        pl.when(lin == total - 1)(lambda: merged.end(cins, couts, csems))

    any_spec = pl.BlockSpec(memory_space=pl.ANY)
    res = pl.pallas_call(
        wrapped, name=name, grid=grid, in_specs=list(in_specs) + [any_spec] * ci,
        out_specs=out_specs_l + [any_spec] * co, out_shape=out_shape_l + merged.out_shapes,
        scratch_shapes=scratch_shapes + merged.sems,
        compiler_params=_params(("arbitrary",) * len(grid)))(*args, *merged.inputs)
    outs, couts = res[:n_out], res[n_out:]
    return (outs[0] if single else list(outs)), [list(couts[a:b]) for a, b in spans]


def _comm_only(comm, name):
    def body():
        pass

    _, couts = _call(body, name=name, grid=(1,), in_specs=[], out_specs=[], out_shape=[], args=[], comm=comm)
    return couts[0]


def _tile(dim, pref):
    best = None
    for t in range(LANE, min(dim, pref) + 1, LANE):
        if dim % t == 0:
            best = t
    return best if best is not None else dim


MM_VMEM_BUDGET = 40 * 1024 * 1024


def _mm_tiles(m, n, k, a_bytes, b_bytes, o_bytes, r_bytes):
    for tk in (k, _tile(k, 2048), _tile(k, 1024), _tile(k, 512)):
        for tm, tn in ((1024, 1536), (1024, 1024), (1024, 512), (512, 512), (256, 512), (256, 256)):
            tm, tn = _tile(m, tm), _tile(n, tn)
            acc = 0 if tk == k else tm * tn * 4
            need = 2 * (tm * tk * a_bytes + tk * tn * b_bytes + tm * tn * (o_bytes + r_bytes)) + acc
            if need <= MM_VMEM_BUDGET:
                return tm, tn, tk
    raise ValueError("no matmul tiling fits VMEM")


def _mm(a, b, mode, name, out_dtype=F32, res=None, comm=None):
    if mode == "nn":
        (m, k), n = a.shape, b.shape[1]
    elif mode == "nt":
        (m, k), n = a.shape, b.shape[0]
    else:
        (k, m), n = a.shape, b.shape[1]
    tm, tn, tk = _mm_tiles(m, n, k, a.dtype.itemsize, b.dtype.itemsize, jnp.dtype(out_dtype).itemsize,
                           0 if res is None else res.dtype.itemsize)
    nk = k // tk
    dims = {"nn": ((1,), (0,)), "nt": ((1,), (1,)), "tn": ((0,), (0,))}[mode]

    def body(*refs):
        if res is None:
            a_ref, b_ref, o_ref = refs[:3]
            r_ref = None
        else:
            a_ref, b_ref, r_ref, o_ref = refs[:4]
        part = lax.dot_general(a_ref[...].astype(BF16), b_ref[...].astype(BF16), (dims, ((), ())),
                               preferred_element_type=F32)

        def finish(out):
            if r_ref is not None:
                out = out + r_ref[...].astype(F32)
            o_ref[...] = out.astype(out_dtype)

        if nk == 1:
            finish(part)
            return
        acc = refs[-1]
        kk = pl.program_id(2)

        @pl.when(kk == 0)
        def _():
            acc[...] = part

        @pl.when(kk > 0)
        def _():
            acc[...] += part

        @pl.when(kk == nk - 1)
        def _():
            finish(acc[...])

    a_spec = (pl.BlockSpec((tk, tm), lambda i, j, q: (q, i)) if mode == "tn"
              else pl.BlockSpec((tm, tk), lambda i, j, q: (i, q)))
    b_spec = (pl.BlockSpec((tn, tk), lambda i, j, q: (j, q)) if mode == "nt"
              else pl.BlockSpec((tk, tn), lambda i, j, q: (q, j)))
    o_spec = pl.BlockSpec((tm, tn), lambda i, j, q: (i, j))
    in_specs, args = [a_spec, b_spec], [a, b]
    if res is not None:
        in_specs.append(o_spec)
        args.append(res)
    return _call(body, name=name, grid=(m // tm, n // tn, nk), in_specs=in_specs, out_specs=o_spec,
                 out_shape=jax.ShapeDtypeStruct((m, n), out_dtype),
                 scratch_shapes=[] if nk == 1 else [pltpu.VMEM((tm, tn), F32)],
                 sem=("parallel", "parallel", "arbitrary"), args=args, comm=comm)


def _mm_bd(a, b, mode, name, out_dtype=F32, res=None, comm=None, tm=1024):
    if mode == "tn":
        k = a.shape[0]
        nb = min(a.shape[1], b.shape[1]) // LANE
        ma, n = a.shape[1] // nb, b.shape[1] // nb

        def body(a_ref, b_ref, o_ref):
            o_ref[0] = lax.dot_general(a_ref[...].astype(BF16), b_ref[...].astype(BF16), (((0,), (0,)), ((), ())),
                                       preferred_element_type=F32).astype(out_dtype)

        return _call(body, name=name, grid=(nb,),
                     in_specs=[pl.BlockSpec((k, ma), lambda j: (0, j)), pl.BlockSpec((k, n), lambda j: (0, j))],
                     out_specs=pl.BlockSpec((1, ma, n), lambda j: (j, 0, 0)),
                     out_shape=jax.ShapeDtypeStruct((nb, ma, n), out_dtype), sem=("parallel",), args=(a, b), comm=comm)
    m = a.shape[0]
    nb = b.shape[0]
    ka = a.shape[1] // nb
    n = b.shape[2] if mode == "nn" else b.shape[1]
    tm = _tile(m, tm)
    dims = ((1,), (0,)) if mode == "nn" else ((1,), (1,))

    def body(*refs):
        if res is None:
            a_ref, b_ref, o_ref = refs
            r_ref = None
        else:
            a_ref, b_ref, r_ref, o_ref = refs
        out = lax.dot_general(a_ref[...].astype(BF16), b_ref[0].astype(BF16), (dims, ((), ())),
                              preferred_element_type=F32)
        if r_ref is not None:
            out = out + r_ref[...].astype(F32)
        o_ref[...] = out.astype(out_dtype)

    o_spec = pl.BlockSpec((tm, n), lambda i, j: (i, j))
    in_specs = [pl.BlockSpec((tm, ka), lambda i, j: (i, j)), pl.BlockSpec((1,) + b.shape[1:], lambda i, j: (j, 0, 0))]
    args = [a, b]
    if res is not None:
        in_specs.append(o_spec)
        args.append(res)
    return _call(body, name=name, grid=(m // tm, nb), in_specs=in_specs, out_specs=o_spec,
                 out_shape=jax.ShapeDtypeStruct((m, nb * n), out_dtype), sem=("parallel", "parallel"),
                 args=args, comm=comm)


def _rms_fwd(x, g, out_dtype, name, tr=256):
    rows, d = x.shape

    def body(x_ref, g_ref, o_ref):
        xv = x_ref[...]
        r = lax.rsqrt(jnp.mean(xv * xv, axis=-1, keepdims=True) + RMS_EPS)
        o_ref[...] = (xv * r * g_ref[...]).astype(out_dtype)

    return pl.pallas_call(
        body, name=name, grid=(rows // tr,),
        in_specs=[pl.BlockSpec((tr, d), lambda i: (i, 0)), pl.BlockSpec((1, d), lambda i: (0, 0))],
        out_specs=pl.BlockSpec((tr, d), lambda i: (i, 0)), out_shape=jax.ShapeDtypeStruct((rows, d), out_dtype),
        compiler_params=_params(("parallel",)))(x, g.reshape(1, d))


def _rms_bwd(x, g, dy, dres, name, tr=256, comm=None):
    rows, d = x.shape

    def body(*refs):
        if dres is None:
            x_ref, g_ref, dy_ref, dx_ref, dg_ref = refs
            r_ref = None
        else:
            x_ref, g_ref, dy_ref, r_ref, dx_ref, dg_ref = refs

        @pl.when(pl.program_id(0) == 0)
        def _():
            dg_ref[...] = jnp.zeros_like(dg_ref)

        xv, dyv = x_ref[...], dy_ref[...].astype(F32)
        r = lax.rsqrt(jnp.mean(xv * xv, axis=-1, keepdims=True) + RMS_EPS)
        xh = xv * r
        dyg = dyv * g_ref[...]
        dx = r * (dyg - xh * jnp.mean(dyg * xh, axis=-1, keepdims=True))
        if r_ref is not None:
            dx = dx + r_ref[...]
        dx_ref[...] = dx
        dg_ref[...] += jnp.sum(dyv * xh, axis=0, keepdims=True)

    blk = pl.BlockSpec((tr, d), lambda i: (i, 0))
    vec = pl.BlockSpec((1, d), lambda i: (0, 0))
    in_specs, args = [blk, vec, blk], [x, g.reshape(1, d), dy]
    if dres is not None:
        in_specs.append(blk)
        args.append(dres)
    return _call(
        body, name=name, grid=(rows // tr,), in_specs=in_specs, out_specs=[blk, vec],
        out_shape=[jax.ShapeDtypeStruct((rows, d), F32), jax.ShapeDtypeStruct((1, d), F32)],
        sem=("arbitrary",), args=args, comm=comm)


def _loss_head(x, g, target, name, tr=256):
    rows, d = x.shape

    def body(x_ref, g_ref, t_ref, loss_ref, dx_ref, dg_ref):
        @pl.when(pl.program_id(0) == 0)
        def _():
            dg_ref[...] = jnp.zeros_like(dg_ref)
            loss_ref[...] = jnp.zeros_like(loss_ref)

        xv = x_ref[...]
        r = lax.rsqrt(jnp.mean(xv * xv, axis=-1, keepdims=True) + RMS_EPS)
        xh = xv * r
        err = xh * g_ref[...] - t_ref[...]
        loss_ref[...] += 0.5 * jnp.sum(jnp.mean(err * err, axis=-1, keepdims=True), keepdims=True)
        dyv = err * (1.0 / d)
        dyg = dyv * g_ref[...]
        dx_ref[...] = r * (dyg - xh * jnp.mean(dyg * xh, axis=-1, keepdims=True))
        dg_ref[...] += jnp.sum(dyv * xh, axis=0, keepdims=True)

    blk = pl.BlockSpec((tr, d), lambda i: (i, 0))
    vec = pl.BlockSpec((1, d), lambda i: (0, 0))
    return pl.pallas_call(
        body, name=name, grid=(rows // tr,), in_specs=[blk, vec, blk],
        out_specs=[pl.BlockSpec((1, 1), lambda i: (0, 0)), blk, vec],
        out_shape=[jax.ShapeDtypeStruct((1, 1), F32), jax.ShapeDtypeStruct((rows, d), F32),
                   jax.ShapeDtypeStruct((1, d), F32)],
        compiler_params=_params(("arbitrary",)))(x, g.reshape(1, d), target)


def _shift_down(x, s):
    rows = lax.broadcasted_iota(jnp.int32, x.shape, 0)
    return jnp.where(rows >= s, pltpu.roll(x, s, 0), 0.0)


def _shift_up(x, s):
    n = x.shape[0]
    rows = lax.broadcasted_iota(jnp.int32, x.shape, 0)
    return jnp.where(rows < n - s, pltpu.roll(x, n - s, 0), 0.0)


def _sigmoid(x):
    return 1.0 / (1.0 + jnp.exp(-x))


def _silu_and_grad(x):
    s = _sigmoid(x)
    return x * s, s * (1.0 + x * (1.0 - s))


_GELU_C0, _GELU_C1 = math.sqrt(2.0 / math.pi), 0.044715


def _gelu_and_grad(x):
    th = jnp.tanh(_GELU_C0 * (x + _GELU_C1 * x * x * x))
    y = 0.5 * x * (1.0 + th)
    dy = 0.5 * (1.0 + th) + 0.5 * x * (1.0 - th * th) * _GELU_C0 * (1.0 + 3.0 * _GELU_C1 * x * x)
    return y, dy


def _ffn_act_fwd(h, w, name, tc=256, comm=None):
    t = h.shape[0]
    nb = D_FF // tc

    def body(hg_ref, hv_ref, wg_ref, wv_ref, a_ref):
        def conv(x, wr):
            return wr[2:3, :] * x + wr[1:2, :] * _shift_down(x, 1) + wr[0:1, :] * _shift_down(x, 2)

        cg = conv(hg_ref[...], wg_ref[...])
        cv = conv(hv_ref[...], wv_ref[...])
        a_ref[...] = (cg * _sigmoid(cg) * cv).astype(BF16)

    return _call(
        body, name=name, grid=(nb,),
        in_specs=[pl.BlockSpec((t, tc), lambda j: (0, j)), pl.BlockSpec((t, tc), lambda j: (0, j + nb)),
                  pl.BlockSpec((CONV_FFN, tc), lambda j: (0, j)), pl.BlockSpec((CONV_FFN, tc), lambda j: (0, j + nb))],
        out_specs=pl.BlockSpec((t, tc), lambda j: (0, j)), out_shape=jax.ShapeDtypeStruct((t, D_FF), BF16),
        sem=("parallel",), args=(h, h, w, w), comm=comm)


def _ffn_act_bwd(h, w, da, name, tc=256, comm=None):
    t = h.shape[0]
    nb = D_FF // tc

    def body(hg_ref, hv_ref, wg_ref, wv_ref, da_ref, dhg_ref, dhv_ref, dwg_ref, dwv_ref):
        hg, hv, wg, wv = hg_ref[...], hv_ref[...], wg_ref[...], wv_ref[...]
        hg1, hg2, hv1, hv2 = _shift_down(hg, 1), _shift_down(hg, 2), _shift_down(hv, 1), _shift_down(hv, 2)
        cg = wg[2:3, :] * hg + wg[1:2, :] * hg1 + wg[0:1, :] * hg2
        cv = wv[2:3, :] * hv + wv[1:2, :] * hv1 + wv[0:1, :] * hv2
        sg, dsg = _silu_and_grad(cg)
        dav = da_ref[...].astype(F32)
        dcv = dav * sg
        dcg = dav * cv * dsg

        def conv_t(dc, wr):
            return wr[2:3, :] * dc + wr[1:2, :] * _shift_up(dc, 1) + wr[0:1, :] * _shift_up(dc, 2)

        dhg_ref[...] = conv_t(dcg, wg).astype(BF16)
        dhv_ref[...] = conv_t(dcv, wv).astype(BF16)
        dwg_ref[0:1, :] = jnp.sum(dcg * hg2, axis=0, keepdims=True)
        dwg_ref[1:2, :] = jnp.sum(dcg * hg1, axis=0, keepdims=True)
        dwg_ref[2:3, :] = jnp.sum(dcg * hg, axis=0, keepdims=True)
        dwv_ref[0:1, :] = jnp.sum(dcv * hv2, axis=0, keepdims=True)
        dwv_ref[1:2, :] = jnp.sum(dcv * hv1, axis=0, keepdims=True)
        dwv_ref[2:3, :] = jnp.sum(dcv * hv, axis=0, keepdims=True)

    big = lambda off: pl.BlockSpec((t, tc), lambda j: (0, j + off))
    small = lambda off: pl.BlockSpec((CONV_FFN, tc), lambda j: (0, j + off))
    res = _call(
        body, name=name, grid=(nb,),
        in_specs=[big(0), big(nb), small(0), small(nb), big(0)],
        out_specs=[big(0), big(0), small(0), small(0)],
        out_shape=[jax.ShapeDtypeStruct((t, D_FF), BF16), jax.ShapeDtypeStruct((t, D_FF), BF16),
                   jax.ShapeDtypeStruct((CONV_FFN, D_FF), F32), jax.ShapeDtypeStruct((CONV_FFN, D_FF), F32)],
        sem=("parallel",), args=(h, h, w, w, da), comm=comm)
    (dhg, dhv, dwg, dwv), couts = res if comm is not None else (res, None)
    out = (jnp.concatenate([dhg, dhv], axis=1), jnp.concatenate([dwg, dwv], axis=1))
    return out if comm is None else (out, couts)


def _attn_probs(q, k):
    s = lax.dot_general(q.astype(BF16), k.astype(BF16), (((1,), (1,)), ((), ())),
                        preferred_element_type=F32) * (HEAD_X ** -0.5)
    s = s - jnp.max(s, axis=-1, keepdims=True)
    p = jnp.exp(s)
    return p / jnp.sum(p, axis=-1, keepdims=True)


def _attn_fwd(q, kv, name, tq=512, comm=None):
    t = q.shape[0]

    def body(q_ref, k_ref, v_ref, o_ref):
        p = _attn_probs(q_ref[...], k_ref[...])
        o_ref[...] = jnp.dot(p.astype(BF16), v_ref[...].astype(BF16), preferred_element_type=F32).astype(BF16)

    return _call(
        body, name=name, grid=(N_HEADS_X, t // tq),
        in_specs=[pl.BlockSpec((tq, HEAD_X), lambda h, i: (i, h)),
                  pl.BlockSpec((MEM_LEN, HEAD_X), lambda h, i: (0, h)),
                  pl.BlockSpec((MEM_LEN, HEAD_X), lambda h, i: (0, h + N_HEADS_X))],
        out_specs=pl.BlockSpec((tq, HEAD_X), lambda h, i: (i, h)),
        out_shape=jax.ShapeDtypeStruct((t, N_HEADS_X * HEAD_X), BF16),
        sem=("parallel", "parallel"), args=(q, kv, kv), comm=comm)


def _attn_bwd(q, kv, do, name, tq=512):
    t = q.shape[0]

    def body(q_ref, k_ref, v_ref, do_ref, dq_ref, dk_ref, dv_ref):
        @pl.when(pl.program_id(1) == 0)
        def _():
            dk_ref[...] = jnp.zeros_like(dk_ref)
            dv_ref[...] = jnp.zeros_like(dv_ref)

        qb, kb, vb, dob = (r[...].astype(BF16) for r in (q_ref, k_ref, v_ref, do_ref))
        p = _attn_probs(qb, kb)
        dp = lax.dot_general(dob, vb, (((1,), (1,)), ((), ())), preferred_element_type=F32)
        ds = p * (dp - jnp.sum(dp * p, axis=-1, keepdims=True)) * (HEAD_X ** -0.5)
        dsb = ds.astype(BF16)
        dq_ref[...] = jnp.dot(dsb, kb, preferred_element_type=F32).astype(BF16)
        dk_ref[...] += lax.dot_general(dsb, qb, (((0,), (0,)), ((), ())), preferred_element_type=F32)
        dv_ref[...] += lax.dot_general(p.astype(BF16), dob, (((0,), (0,)), ((), ())), preferred_element_type=F32)

    qs = pl.BlockSpec((tq, HEAD_X), lambda h, i: (i, h))
    ms = pl.BlockSpec((MEM_LEN, HEAD_X), lambda h, i: (0, h))
    return pl.pallas_call(
        body, name=name, grid=(N_HEADS_X, t // tq),
        in_specs=[qs, ms, pl.BlockSpec((MEM_LEN, HEAD_X), lambda h, i: (0, h + N_HEADS_X)), qs],
        out_specs=[qs, ms, ms],
        out_shape=[jax.ShapeDtypeStruct((t, D_MODEL), BF16), jax.ShapeDtypeStruct((MEM_LEN, D_MODEL), F32),
                   jax.ShapeDtypeStruct((MEM_LEN, D_MODEL), F32)],
        compiler_params=_params(("parallel", "arbitrary")))(q, kv, kv, do)


def _pool_counts(t, win):
    pos = lax.broadcasted_iota(jnp.int32, (t, 1), 0).astype(F32) + 1.0
    return 1.0 / jnp.minimum(pos, float(win))


def _pool_delta(xv, win):
    s, step = xv, 1
    while step < win:
        s = s + _shift_down(s, step)
        step *= 2
    return s * _pool_counts(xv.shape[0], win) - xv


def _pool_delta_t(dv, win):
    s, step = dv * _pool_counts(dv.shape[0], win), 1
    while step < win:
        s = s + _shift_up(s, step)
        step *= 2
    return s - dv


def _pool_fwd(xn, w, scale, res, name):
    t = xn.shape[0]

    def make_branch(win, xn_ref, w_ref, s_ref, r_ref, o_ref):
        def branch():
            dl = _pool_delta(xn_ref[...], win)
            y = jnp.dot(dl.astype(BF16), w_ref[0], preferred_element_type=F32)
            o_ref[...] = r_ref[...] + y * s_ref[...]
        return branch

    def body(xn_ref, w_ref, s_ref, r_ref, o_ref):
        for gi, win in enumerate(POOL_WINDOWS):
            pl.when(pl.program_id(0) == gi)(make_branch(win, xn_ref, w_ref, s_ref, r_ref, o_ref))

    blk = pl.BlockSpec((t, POOL_GROUP), lambda g: (0, g))
    return pl.pallas_call(
        body, name=name, grid=(len(POOL_WINDOWS),),
        in_specs=[blk, pl.BlockSpec((1, POOL_GROUP, POOL_GROUP), lambda g: (g, 0, 0)),
                  pl.BlockSpec((1, POOL_GROUP), lambda g: (0, g)), blk],
        out_specs=blk, out_shape=jax.ShapeDtypeStruct((t, D_MODEL), F32),
        compiler_params=_params(("parallel",)))(xn, w, scale, res)


def _pool_bwd(xn, w, scale, dmix, name):
    t = xn.shape[0]

    def make_branch(win, xn_ref, w_ref, s_ref, d_ref, dxn_ref, dw_ref, ds_ref):
        def branch():
            dl = _pool_delta(xn_ref[...], win).astype(BF16)
            wv = w_ref[0]
            dm = d_ref[...]
            y = jnp.dot(dl, wv, preferred_element_type=F32)
            ds_ref[...] = jnp.sum(dm * y, axis=0, keepdims=True)
            dy = (dm * s_ref[...]).astype(BF16)
            dw_ref[0] = lax.dot_general(dl, dy, (((0,), (0,)), ((), ())), preferred_element_type=F32)
            ddl = lax.dot_general(dy, wv, (((1,), (1,)), ((), ())), preferred_element_type=F32)
            dxn_ref[...] = _pool_delta_t(ddl, win)
        return branch

    def body(*refs):
        for gi, win in enumerate(POOL_WINDOWS):
            pl.when(pl.program_id(0) == gi)(make_branch(win, *refs))

    blk = pl.BlockSpec((t, POOL_GROUP), lambda g: (0, g))
    wspec = pl.BlockSpec((1, POOL_GROUP, POOL_GROUP), lambda g: (g, 0, 0))
    vec = pl.BlockSpec((1, POOL_GROUP), lambda g: (0, g))
    return pl.pallas_call(
        body, name=name, grid=(len(POOL_WINDOWS),), in_specs=[blk, wspec, vec, blk], out_specs=[blk, wspec, vec],
        out_shape=[jax.ShapeDtypeStruct((t, D_MODEL), F32),
                   jax.ShapeDtypeStruct((len(POOL_WINDOWS), POOL_GROUP, POOL_GROUP), F32),
                   jax.ShapeDtypeStruct((1, D_MODEL), F32)],
        compiler_params=_params(("parallel",)))(xn, w, scale, dmix)


def _qkv_conv(h, wr):
    return (wr[3:4, :] * h + wr[2:3, :] * _shift_down(h, 1) + wr[1:2, :] * _shift_down(h, 2)
            + wr[0:1, :] * _shift_down(h, 3))


def _qkv_pre_fwd(h, w, col0, ncols, normalize, scale, name):
    t = h.shape[0]

    def body(h_ref, w_ref, o_ref):
        c = _qkv_conv(h_ref[...], w_ref[...])
        s = c * _sigmoid(c)
        if normalize:
            s = s * lax.rsqrt(jnp.sum(s * s, axis=-1, keepdims=True) + 1e-6) * scale
        o_ref[...] = s

    return pl.pallas_call(
        body, name=name, grid=(ncols,),
        in_specs=[pl.BlockSpec((t, HEAD_A), lambda j: (0, j + col0)), pl.BlockSpec((CONV_A, HEAD_A), lambda j: (0, j + col0))],
        out_specs=pl.BlockSpec((t, HEAD_A), lambda j: (0, j)), out_shape=jax.ShapeDtypeStruct((t, ncols * HEAD_A), F32),
        compiler_params=_params(("parallel",)))(h, w)


def _qkv_pre_bwd(h, w, dy, col0, ncols, normalize, scale, name):
    t = h.shape[0]

    def body(h_ref, w_ref, dy_ref, dh_ref, dw_ref):
        hv, wr, dyv = h_ref[...], w_ref[...], dy_ref[...]
        h1, h2, h3 = _shift_down(hv, 1), _shift_down(hv, 2), _shift_down(hv, 3)
        c = wr[3:4, :] * hv + wr[2:3, :] * h1 + wr[1:2, :] * h2 + wr[0:1, :] * h3
        s, dsilu = _silu_and_grad(c)
        if normalize:
            r = lax.rsqrt(jnp.sum(s * s, axis=-1, keepdims=True) + 1e-6)
            y = s * r
            dyv = dyv * scale
            ds = r * (dyv - y * jnp.sum(dyv * y, axis=-1, keepdims=True))
        else:
            ds = dyv
        dc = ds * dsilu
        dh = (wr[3:4, :] * dc + wr[2:3, :] * _shift_up(dc, 1) + wr[1:2, :] * _shift_up(dc, 2)
              + wr[0:1, :] * _shift_up(dc, 3))
        dh_ref[...] = dh.astype(BF16)
        dw_ref[0:1, :] = jnp.sum(dc * h3, axis=0, keepdims=True)
        dw_ref[1:2, :] = jnp.sum(dc * h2, axis=0, keepdims=True)
        dw_ref[2:3, :] = jnp.sum(dc * h1, axis=0, keepdims=True)
        dw_ref[3:4, :] = jnp.sum(dc * hv, axis=0, keepdims=True)

    return pl.pallas_call(
        body, name=name, grid=(ncols,),
        in_specs=[pl.BlockSpec((t, HEAD_A), lambda j: (0, j + col0)), pl.BlockSpec((CONV_A, HEAD_A), lambda j: (0, j + col0)),
                  pl.BlockSpec((t, HEAD_A), lambda j: (0, j))],
        out_specs=[pl.BlockSpec((t, HEAD_A), lambda j: (0, j)), pl.BlockSpec((CONV_A, HEAD_A), lambda j: (0, j))],
        out_shape=[jax.ShapeDtypeStruct((t, ncols * HEAD_A), BF16), jax.ShapeDtypeStruct((CONV_A, ncols * HEAD_A), F32)],
        compiler_params=_params(("parallel",)))(h, w, dy)


def _softplus(x):
    return jnp.maximum(x, 0.0) + jnp.log1p(jnp.exp(-jnp.abs(x)))


def _gates_fwd(ba, arow, brow, name):
    t = ba.shape[0]

    def body(x_ref, a_ref, b_ref, o_ref):
        xv = x_ref[...]
        lane = lax.broadcasted_iota(jnp.int32, xv.shape, 1)
        beta = _sigmoid(xv)
        g = -jnp.exp(a_ref[...]) * _softplus(xv + b_ref[...])
        o_ref[...] = jnp.where(lane < N_HEADS_A, beta, jnp.where(lane < 2 * N_HEADS_A, g, 0.0))

    return pl.pallas_call(body, name=name, out_shape=jax.ShapeDtypeStruct((t, LANE), F32),
                          compiler_params=_params())(ba, arow, brow)


def _gates_bwd(ba, arow, brow, dgb, name):
    t = ba.shape[0]

    def body(x_ref, a_ref, b_ref, d_ref, dx_ref, da_ref, db_ref):
        xv = x_ref[...]
        dv = d_ref[0] + d_ref[1] + d_ref[2] + d_ref[3]
        lane = lax.broadcasted_iota(jnp.int32, xv.shape, 1)
        beta = _sigmoid(xv)
        ea = jnp.exp(a_ref[...])
        z = xv + b_ref[...]
        dgv = jnp.where((lane >= N_HEADS_A) & (lane < 2 * N_HEADS_A), dv, 0.0) * (-ea)
        dz = dgv * _sigmoid(z)
        dx = jnp.where(lane < N_HEADS_A, dv * beta * (1.0 - beta), dz)
        dx_ref[...] = dx.astype(BF16)
        db_ref[...] = jnp.sum(dz, axis=0, keepdims=True)
        da_ref[...] = jnp.sum(dgv * _softplus(z), axis=0, keepdims=True)

    return pl.pallas_call(
        body, name=name,
        out_shape=[jax.ShapeDtypeStruct((t, LANE), BF16), jax.ShapeDtypeStruct((1, LANE), F32),
                   jax.ShapeDtypeStruct((1, LANE), F32)],
        compiler_params=_params())(ba, arow, brow, dgb)


def _dot(a, b, prec=None):
    if prec is None:
        return jnp.dot(a.astype(BF16), b.astype(BF16), preferred_element_type=F32)
    return jnp.dot(a, b, precision=prec, preferred_element_type=F32)


def _dot_nt(a, b, prec=None):
    if prec is None:
        a, b = a.astype(BF16), b.astype(BF16)
    return lax.dot_general(a, b, (((1,), (1,)), ((), ())), precision=prec, preferred_element_type=F32)


def _dot_tn(a, b, prec=None):
    if prec is None:
        a, b = a.astype(BF16), b.astype(BF16)
    return lax.dot_general(a, b, (((0,), (0,)), ((), ())), precision=prec, preferred_element_type=F32)


def _gdr_chunk_terms(k, beta, g):
    c = GDR_CHUNK
    row = lax.broadcasted_iota(jnp.int32, (c, c), 0)
    col = lax.broadcasted_iota(jnp.int32, (c, c), 1)
    causal, strict = row >= col, row > col
    gcum = _dot(causal.astype(F32), jnp.broadcast_to(g, (c, c)), HIGHEST)
    diff = gcum - gcum.T
    decay = jnp.where(causal, jnp.exp(jnp.where(causal, diff, 0.0)), 0.0)
    kb = k * beta
    kk = _dot_nt(kb, k)
    return row, col, causal, strict, gcum, decay, kb, kk


def _unit_lower_inverse(a):
    c = a.shape[0]
    eye = (lax.broadcasted_iota(jnp.int32, (c, c), 0) == lax.broadcasted_iota(jnp.int32, (c, c), 1)).astype(F32)
    p = -a
    inv = eye + p
    step = 1
    while 2 * step < c:
        p = _dot(p, p, HIGH)
        inv = inv + _dot(inv, p, HIGH)
        step *= 2
    return inv


def _head_gates(gates, head):
    lane = lax.broadcasted_iota(jnp.int32, gates.shape, 1)
    beta = jnp.sum(jnp.where(lane == head, gates, 0.0), axis=1, keepdims=True)
    g = jnp.sum(jnp.where(lane == head + N_HEADS_A, gates, 0.0), axis=1, keepdims=True)
    return beta, g


def _gdr_fwd(q, k, v, gates, name, comm=None):
    t = q.shape[0]
    c = GDR_CHUNK
    n = t // c

    hps = GDR_HEADS_PER_STEP

    def one_head(hh, q_ref, k_ref, v_ref, gb_ref, o_ref, tm_ref, s_ref, state):
        cols = slice(hh * HEAD_A, (hh + 1) * HEAD_A)
        qv, kv, vv = q_ref[:, cols], k_ref[:, cols], v_ref[:, cols]
        beta, g = _head_gates(gb_ref[...], pl.program_id(0) * hps + hh)
        row, col, causal, strict, gcum, decay, kb, kk = _gdr_chunk_terms(kv, beta, g)
        tm = _unit_lower_inverse(jnp.where(strict, kk * decay, 0.0))
        e = jnp.exp(gcum)
        u = _dot(tm, vv * beta, HIGH)
        w = _dot(tm, kb * e, HIGH)
        p = jnp.where(causal, _dot_nt(qv, kv) * decay, 0.0)
        s = state[hh]
        s_ref[hh, 0] = s
        tm_ref[hh, 0] = tm
        vn = u - _dot(w, s)
        o_ref[:, cols] = _dot(qv * e, s) + _dot(p, vn)
        glast = gcum[c - 1:c, :]
        state[hh] = s * jnp.exp(glast) + _dot_tn(kv * jnp.exp(glast - gcum), vn)

    def body(*refs):
        state = refs[-1]

        @pl.when(pl.program_id(1) == 0)
        def _():
            state[...] = jnp.zeros_like(state)

        for hh in range(hps):
            one_head(hh, *refs)

    blk = pl.BlockSpec((c, hps * HEAD_A), lambda h, i: (i, h))
    mat = pl.BlockSpec((hps, 1, c, c), lambda h, i: (h, i, 0, 0))
    return _call(
        body, name=name, grid=(N_HEADS_A // hps, n),
        in_specs=[blk, blk, blk, pl.BlockSpec((c, LANE), lambda h, i: (i, 0))],
        out_specs=[blk, mat, mat],
        out_shape=[jax.ShapeDtypeStruct((t, WIDTH_A), F32), jax.ShapeDtypeStruct((N_HEADS_A, n, c, c), F32),
                   jax.ShapeDtypeStruct((N_HEADS_A, n, HEAD_A, HEAD_A), F32)],
        scratch_shapes=[pltpu.VMEM((hps, HEAD_A, HEAD_A), F32)], sem=("parallel", "arbitrary"),
        args=(q, k, v, gates), comm=comm)


def _gdr_bwd(q, k, v, gates, tm_all, s_all, do, name, comm=None):
    t = q.shape[0]
    c = GDR_CHUNK
    n = t // c

    hps = GDR_HEADS_PER_STEP

    def one_head(hh, q_ref, k_ref, v_ref, gb_ref, tm_ref, s_ref, do_ref, dq_ref, dk_ref, dv_ref, dgb_ref, dstate):
        cols = slice(hh * HEAD_A, (hh + 1) * HEAD_A)
        qv, kv, vv, dov = q_ref[:, cols], k_ref[:, cols], v_ref[:, cols], do_ref[:, cols]
        head = pl.program_id(0) * hps + hh
        beta, g = _head_gates(gb_ref[...], head)
        tm, s, dsp = tm_ref[hh, 0], s_ref[hh, 0], dstate[hh]
        row, col, causal, strict, gcum, decay, kb, kk = _gdr_chunk_terms(kv, beta, g)
        e = jnp.exp(gcum)
        vb, kbe = vv * beta, kb * e
        u = _dot(tm, vb, HIGH)
        w = _dot(tm, kbe, HIGH)
        qk = _dot_nt(qv, kv)
        p = jnp.where(causal, qk * decay, 0.0)
        vn = u - _dot(w, s)
        glast = gcum[c - 1:c, :]
        el = jnp.exp(glast)
        f = jnp.exp(glast - gcum)
        kd = kv * f
        qe = qv * e

        dvn = _dot_tn(p, dov) + _dot(kd, dsp)
        dglast = el[:, 0:1] * jnp.sum(s * dsp, keepdims=True)
        dkd = _dot_nt(vn, dsp)
        dk = dkd * f
        df = jnp.sum(dkd * kv, axis=1, keepdims=True) * f[:, 0:1]
        dglast = dglast + jnp.sum(df, keepdims=True)
        dgc = -df
        dp = jnp.where(causal, _dot_nt(dov, vn), 0.0)
        dqe = _dot_nt(dov, s)
        dq = dqe * e
        de = jnp.sum(dqe * qv, axis=1, keepdims=True)
        dstate[hh] = dsp * el + _dot_tn(qe, dov) - _dot_tn(w, dvn)
        dw = -_dot_nt(dvn, s)
        dvb = _dot_tn(tm, dvn, HIGH)
        dkbe = _dot_tn(tm, dw, HIGH)
        da = -jnp.where(strict, _dot_nt(dvb, u) + _dot_nt(dkbe, w), 0.0)
        dkk = da * decay
        dqk = dp * decay
        dd = da * kk + dp * qk
        dq = dq + _dot(dqk, kv)
        dk = dk + _dot_tn(dqk, qv)
        dkb = _dot(dkk, kv) + dkbe * e
        dk = dk + _dot_tn(dkk, kb)
        de = de + jnp.sum(dkbe * kb, axis=1, keepdims=True)
        dk = dk + dkb * beta
        dbeta = jnp.sum(dkb * kv, axis=1, keepdims=True) + jnp.sum(dvb * vv, axis=1, keepdims=True)
        m = dd * decay
        dgc = dgc + jnp.sum(m, axis=1, keepdims=True) - jnp.sum(m.T, axis=1, keepdims=True)
        dgc = dgc + de * e[:, 0:1]
        dgc = dgc + jnp.where(row[:, 0:1] == c - 1, dglast, 0.0)
        dg = _dot((row <= col).astype(F32), jnp.broadcast_to(dgc, (c, c)), HIGHEST)
        dq_ref[:, cols] = dq
        dk_ref[:, cols] = dk
        dv_ref[:, cols] = dvb * beta
        lane = lax.broadcasted_iota(jnp.int32, (c, LANE), 1)
        dgb_ref[hh] = jnp.where(lane == head, dbeta, jnp.where(lane == head + N_HEADS_A, dg, 0.0))

    def body(*refs):
        dstate = refs[-1]

        @pl.when(pl.program_id(1) == 0)
        def _():
            dstate[...] = jnp.zeros_like(dstate)

        for hh in range(hps):
            one_head(hh, *refs)

    blk = pl.BlockSpec((c, hps * HEAD_A), lambda h, i: (n - 1 - i, h))
    mat = pl.BlockSpec((hps, 1, c, c), lambda h, i: (h, n - 1 - i, 0, 0))
    return _call(
        body, name=name, grid=(N_HEADS_A // hps, n),
        in_specs=[blk, blk, blk, pl.BlockSpec((c, LANE), lambda h, i: (n - 1 - i, 0)), mat, mat, blk],
        out_specs=[blk, blk, blk, pl.BlockSpec((hps, c, LANE), lambda h, i: (h, n - 1 - i, 0))],
        out_shape=[jax.ShapeDtypeStruct((t, WIDTH_A), F32)] * 3 + [jax.ShapeDtypeStruct((N_HEADS_A, t, LANE), F32)],
        scratch_shapes=[pltpu.VMEM((hps, HEAD_A, HEAD_A), F32)], sem=("parallel", "arbitrary"),
        args=(q, k, v, gates, tm_all, s_all, do), comm=comm)


_B_NN, _B_NT, _B_TN = ((2,), (1,)), ((2,), (2,)), ((1,), (1,))


def _bdot(a, b, dims=_B_NN, prec=None):
    if prec is None:
        a, b = a.astype(BF16), b.astype(BF16)
    return lax.dot_general(a, b, (dims, ((0,), (0,))), precision=prec, preferred_element_type=F32)


def _heads_of(ref):
    return jnp.stack([ref[:, h * HEAD_A:(h + 1) * HEAD_A] for h in range(N_HEADS_A)])


def _all_head_gates(gates):
    pairs = [_head_gates(gates, h) for h in range(N_HEADS_A)]
    return jnp.stack([b for b, _ in pairs]), jnp.stack([g for _, g in pairs])


def _gdr_terms(k, beta, g):
    h, c = k.shape[0], GDR_CHUNK
    row = lax.broadcasted_iota(jnp.int32, (c, c), 0)
    col = lax.broadcasted_iota(jnp.int32, (c, c), 1)
    causal, strict = row >= col, row > col
    lower = jnp.broadcast_to(causal.astype(F32), (h, c, c))
    gcum = _bdot(lower, jnp.broadcast_to(g, (h, c, c)), prec=HIGHEST)
    diff = gcum - jnp.swapaxes(gcum, 1, 2)
    decay = jnp.where(causal, jnp.exp(jnp.where(causal, diff, 0.0)), 0.0)
    kb = k * beta
    return row, col, causal, strict, gcum, decay, kb, _bdot(kb, k, _B_NT)


def _unit_lower_inverses(a):
    c = a.shape[1]
    eye = (lax.broadcasted_iota(jnp.int32, (c, c), 0) == lax.broadcasted_iota(jnp.int32, (c, c), 1)).astype(F32)
    p = -a
    inv = eye + p
    step = 1
    while 2 * step < c:
        p = _bdot(p, p, prec=HIGH)
        inv = inv + _bdot(inv, p, prec=HIGH)
        step *= 2
    return inv


def _gdr_fwd(q, k, v, gates, name, comm=None):
    t = q.shape[0]
    c, nh = GDR_CHUNK, N_HEADS_A
    n = t // c

    def body(q_ref, k_ref, v_ref, gb_ref, o_ref, tm_ref, s_ref, state):
        @pl.when(pl.program_id(0) == 0)
        def _():
            state[...] = jnp.zeros_like(state)

        qv, kv, vv = _heads_of(q_ref), _heads_of(k_ref), _heads_of(v_ref)
        beta, g = _all_head_gates(gb_ref[...])
        row, col, causal, strict, gcum, decay, kb, kk = _gdr_terms(kv, beta, g)
        tm = _unit_lower_inverses(jnp.where(strict, kk * decay, 0.0))
        e = jnp.exp(gcum)
        u = _bdot(tm, vv * beta, prec=HIGH)
        w = _bdot(tm, kb * e, prec=HIGH)
        p = jnp.where(causal, _bdot(qv, kv, _B_NT) * decay, 0.0)
        s = state[...]
        s_ref[:, 0] = s
        tm_ref[:, 0] = tm
        vn = u - _bdot(w, s)
        o = _bdot(qv * e, s) + _bdot(p, vn)
        for h in range(nh):
            o_ref[:, h * HEAD_A:(h + 1) * HEAD_A] = o[h]
        glast = gcum[:, c - 1:c, :]
        state[...] = s * jnp.exp(glast) + _bdot(kv * jnp.exp(glast - gcum), vn, _B_TN)

    blk = pl.BlockSpec((c, WIDTH_A), lambda i: (i, 0))
    mat = pl.BlockSpec((nh, 1, c, c), lambda i: (0, i, 0, 0))
    return _call(
        body, name=name, grid=(n,), in_specs=[blk, blk, blk, pl.BlockSpec((c, LANE), lambda i: (i, 0))],
        out_specs=[blk, mat, mat],
        out_shape=[jax.ShapeDtypeStruct((t, WIDTH_A), F32), jax.ShapeDtypeStruct((nh, n, c, c), F32),
                   jax.ShapeDtypeStruct((nh, n, HEAD_A, HEAD_A), F32)],
        scratch_shapes=[pltpu.VMEM((nh, HEAD_A, HEAD_A), F32)], sem=("arbitrary",),
        args=(q, k, v, gates), comm=comm)


def _gdr_bwd(q, k, v, gates, tm_all, s_all, do, name, comm=None):
    t = q.shape[0]
    c, nh = GDR_CHUNK, N_HEADS_A
    n = t // c

    def body(q_ref, k_ref, v_ref, gb_ref, tm_ref, s_ref, do_ref, dq_ref, dk_ref, dv_ref, dgb_ref, dstate):
        @pl.when(pl.program_id(0) == 0)
        def _():
            dstate[...] = jnp.zeros_like(dstate)

        qv, kv, vv, dov = _heads_of(q_ref), _heads_of(k_ref), _heads_of(v_ref), _heads_of(do_ref)
        beta, g = _all_head_gates(gb_ref[...])
        tm, s, dsp = tm_ref[:, 0], s_ref[:, 0], dstate[...]
        row, col, causal, strict, gcum, decay, kb, kk = _gdr_terms(kv, beta, g)
        rowsum = lambda x: jnp.sum(x, axis=2, keepdims=True)
        e = jnp.exp(gcum)
        vb, kbe = vv * beta, kb * e
        u = _bdot(tm, vb, prec=HIGH)
        w = _bdot(tm, kbe, prec=HIGH)
        qk = _bdot(qv, kv, _B_NT)
        p = jnp.where(causal, qk * decay, 0.0)
        vn = u - _bdot(w, s)
        glast = gcum[:, c - 1:c, :]
        el = jnp.exp(glast)
        f = jnp.exp(glast - gcum)
        kd = kv * f
        qe = qv * e

        dvn = _bdot(p, dov, _B_TN) + _bdot(kd, dsp)
        dglast = el[:, :, 0:1] * jnp.sum(s * dsp, axis=(1, 2), keepdims=True)
        dkd = _bdot(vn, dsp, _B_NT)
        dk = dkd * f
        df = rowsum(dkd * kv) * f[:, :, 0:1]
        dglast = dglast + jnp.sum(df, axis=1, keepdims=True)
        dgc = -df
        dp = jnp.where(causal, _bdot(dov, vn, _B_NT), 0.0)
        dqe = _bdot(dov, s, _B_NT)
        dq = dqe * e
        de = rowsum(dqe * qv)
        dstate[...] = dsp * el + _bdot(qe, dov, _B_TN) - _bdot(w, dvn, _B_TN)
        dw = -_bdot(dvn, s, _B_NT)
        dvb = _bdot(tm, dvn, _B_TN, prec=HIGH)
        dkbe = _bdot(tm, dw, _B_TN, prec=HIGH)
        da = -jnp.where(strict, _bdot(dvb, u, _B_NT) + _bdot(dkbe, w, _B_NT), 0.0)
        dkk = da * decay
        dqk = dp * decay
        dd = da * kk + dp * qk
        dq = dq + _bdot(dqk, kv)
        dk = dk + _bdot(dqk, qv, _B_TN)
        dkb = _bdot(dkk, kv) + dkbe * e
        dk = dk + _bdot(dkk, kb, _B_TN)
        de = de + rowsum(dkbe * kb)
        dk = dk + dkb * beta
        dbeta = rowsum(dkb * kv) + rowsum(dvb * vv)
        m = dd * decay
        dgc = dgc + rowsum(m) - rowsum(jnp.swapaxes(m, 1, 2))
        dgc = dgc + de * e[:, :, 0:1]
        dgc = dgc + jnp.where(row[:, 0:1] == c - 1, dglast, 0.0)
        upper = jnp.broadcast_to((row <= col).astype(F32), (nh, c, c))
        dg = _bdot(upper, jnp.broadcast_to(dgc, (nh, c, c)), prec=HIGHEST)
        dv = dvb * beta
        for h in range(nh):
            cols = slice(h * HEAD_A, (h + 1) * HEAD_A)
            dq_ref[:, cols] = dq[h]
            dk_ref[:, cols] = dk[h]
            dv_ref[:, cols] = dv[h]
        head = lax.broadcasted_iota(jnp.int32, (nh, c, LANE), 0)
        lane = lax.broadcasted_iota(jnp.int32, (nh, c, LANE), 2)
        dgb_ref[...] = jnp.where(lane == head, dbeta, jnp.where(lane == head + nh, dg, 0.0))

    blk = pl.BlockSpec((c, WIDTH_A), lambda i: (n - 1 - i, 0))
    mat = pl.BlockSpec((nh, 1, c, c), lambda i: (0, n - 1 - i, 0, 0))
    return _call(
        body, name=name, grid=(n,),
        in_specs=[blk, blk, blk, pl.BlockSpec((c, LANE), lambda i: (n - 1 - i, 0)), mat, mat, blk],
        out_specs=[blk, blk, blk, pl.BlockSpec((nh, c, LANE), lambda i: (0, n - 1 - i, 0))],
        out_shape=[jax.ShapeDtypeStruct((t, WIDTH_A), F32)] * 3 + [jax.ShapeDtypeStruct((nh, t, LANE), F32)],
        scratch_shapes=[pltpu.VMEM((nh, HEAD_A, HEAD_A), F32)], sem=("arbitrary",),
        args=(q, k, v, gates, tm_all, s_all, do), comm=comm)


def _onorm_fwd(o, gate, g, name):
    t = o.shape[0]

    def body(o_ref, gate_ref, g_ref, y_ref):
        ov, gv = o_ref[...], gate_ref[...]
        r = lax.rsqrt(jnp.mean(ov * ov, axis=-1, keepdims=True) + RMS_EPS)
        y_ref[...] = (ov * r * g_ref[...] * gv * _sigmoid(gv)).astype(BF16)

    blk = pl.BlockSpec((t, HEAD_A), lambda j: (0, j))
    return pl.pallas_call(
        body, name=name, grid=(N_HEADS_A,), in_specs=[blk, blk, pl.BlockSpec((1, HEAD_A), lambda j: (0, 0))],
        out_specs=blk, out_shape=jax.ShapeDtypeStruct((t, WIDTH_A), BF16),
        compiler_params=_params(("parallel",)))(o, gate, g)


def _onorm_bwd(o, gate, g, dy, name):
    t = o.shape[0]

    def body(o_ref, gate_ref, g_ref, dy_ref, do_ref, dgate_ref, dg_ref):
        @pl.when(pl.program_id(0) == 0)
        def _():
            dg_ref[...] = jnp.zeros_like(dg_ref)

        ov, gv, dyv = o_ref[...], gate_ref[...], dy_ref[...].astype(F32)
        r = lax.rsqrt(jnp.mean(ov * ov, axis=-1, keepdims=True) + RMS_EPS)
        oh = ov * r
        sg, dsg = _silu_and_grad(gv)
        dgate_ref[...] = (dyv * oh * g_ref[...] * dsg).astype(BF16)
        dn = dyv * sg
        dg_ref[...] += jnp.sum(dn * oh, axis=0, keepdims=True)
        dng = dn * g_ref[...]
        do_ref[...] = r * (dng - oh * jnp.mean(dng * oh, axis=-1, keepdims=True))

    blk = pl.BlockSpec((t, HEAD_A), lambda j: (0, j))
    vec = pl.BlockSpec((1, HEAD_A), lambda j: (0, 0))
    return pl.pallas_call(
        body, name=name, grid=(N_HEADS_A,), in_specs=[blk, blk, vec, blk], out_specs=[blk, blk, vec],
        out_shape=[jax.ShapeDtypeStruct((t, WIDTH_A), F32), jax.ShapeDtypeStruct((t, WIDTH_A), BF16),
                   jax.ShapeDtypeStruct((1, HEAD_A), F32)],
        compiler_params=_params(("arbitrary",)))(o, gate, g, dy)


def _cmul(ar, ai, br, bi):
    return ar * br - ai * bi, ar * bi + ai * br


def _scan_tables(ar, ai, reverse):
    p1 = (ar, ai)
    p2 = _cmul(*p1, *p1)
    p4 = _cmul(*p2, *p2)
    p8 = _cmul(*p4, *p4)
    p3 = _cmul(*p2, *p1)
    p5 = _cmul(*p4, *p1)
    p6 = _cmul(*p4, *p2)
    p7 = _cmul(*p4, *p3)
    pows = [p1, p2, p3, p4, p5, p6, p7, p8]
    rows = lax.broadcasted_iota(jnp.int32, (8, ar.shape[1]), 0)
    tr = jnp.zeros((8, ar.shape[1]), F32)
    ti = jnp.zeros((8, ar.shape[1]), F32)
    for r in range(8):
        pw = pows[7 - r] if reverse else pows[r]
        tr = jnp.where(rows == r, pw[0], tr)
        ti = jnp.where(rows == r, pw[1], ti)
    return p1, p2, p4, p8, tr, ti


def _tile_scan(xr, xi, p1, p2, p4, reverse):
    rows = lax.broadcasted_iota(jnp.int32, xr.shape, 0)
    for s, (pr, pi) in ((1, p1), (2, p2), (4, p4)):
        if reverse:
            keep = rows < 8 - s
            sr, si = pltpu.roll(xr, 8 - s, 0), pltpu.roll(xi, 8 - s, 0)
        else:
            keep = rows >= s
            sr, si = pltpu.roll(xr, s, 0), pltpu.roll(xi, s, 0)
        sr, si = jnp.where(keep, sr, 0.0), jnp.where(keep, si, 0.0)
        mr, mi = _cmul(pr, pi, sr, si)
        xr, xi = xr + mr, xi + mi
    return xr, xi


def _s5_scan_fwd(bu, a, name, tb=512, comm=None):
    t = bu.shape[0]
    cb = SCAN_CB
    nt = t // tb

    def body(b_ref, a_ref, x_ref, carry):
        @pl.when(pl.program_id(1) == 0)
        def _():
            carry[...] = jnp.zeros_like(carry)

        ar, ai = a_ref[:, 0:cb], a_ref[:, cb:2 * cb]
        p1, p2, p4, p8, tr, ti = _scan_tables(ar, ai, False)

        def step(j, c):
            cr, ci = c
            i = pl.multiple_of(j * 8, 8)
            xr, xi = _tile_scan(b_ref[pl.ds(i, 8), 0:cb], b_ref[pl.ds(i, 8), cb:2 * cb], p1, p2, p4, False)
            mr, mi = _cmul(tr, ti, cr, ci)
            xr, xi = xr + mr, xi + mi
            x_ref[pl.ds(i, 8), 0:cb] = xr
            x_ref[pl.ds(i, 8), cb:2 * cb] = xi
            return xr[7:8, :], xi[7:8, :]

        cr, ci = lax.fori_loop(0, tb // 8, step, (carry[0:1, :], carry[1:2, :]), unroll=2)
        carry[0:1, :] = cr
        carry[1:2, :] = ci

    blk = pl.BlockSpec((tb, 2 * cb), lambda j, i: (i, j))
    return _call(
        body, name=name, grid=(SSM_CH // cb, nt),
        in_specs=[blk, pl.BlockSpec((1, 2 * cb), lambda j, i: (0, j))], out_specs=blk,
        out_shape=jax.ShapeDtypeStruct((t, 2 * SSM_CH), F32), scratch_shapes=[pltpu.VMEM((8, cb), F32)],
        sem=("parallel", "arbitrary"), args=(bu, a), comm=comm)


def _s5_scan_bwd(dx, x, a, name, tb=512, comm=None):
    t = dx.shape[0]
    cb = SCAN_CB
    nt = t // tb
    nj = tb // 8

    def body(d_ref, x_ref, xp_ref, a_ref, l_ref, da_ref, carry, acc):
        tblk = pl.program_id(1)

        @pl.when(tblk == 0)
        def _():
            carry[...] = jnp.zeros_like(carry)
            acc[...] = jnp.zeros_like(acc)

        ar, ai = a_ref[:, 0:cb], a_ref[:, cb:2 * cb]
        p1, p2, p4, p8, tr, ti = _scan_tables(ar, -ai, True)
        rows = lax.broadcasted_iota(jnp.int32, (8, cb), 0)

        def step(jj, c):
            cr, ci, sr_acc, si_acc = c
            j = nj - 1 - jj
            i = pl.multiple_of(j * 8, 8)
            lr, li = _tile_scan(d_ref[pl.ds(i, 8), 0:cb], d_ref[pl.ds(i, 8), cb:2 * cb], p1, p2, p4, True)
            mr, mi = _cmul(tr, ti, cr, ci)
            lr, li = lr + mr, li + mi
            l_ref[pl.ds(i, 8), 0:cb] = lr
            l_ref[pl.ds(i, 8), cb:2 * cb] = li
            ip = pl.multiple_of(jnp.maximum(j - 1, 0) * 8, 8)
            prev_r = jnp.where(j > 0, x_ref[pl.ds(ip, 8), 0:cb], xp_ref[:, 0:cb])
            prev_i = jnp.where(j > 0, x_ref[pl.ds(ip, 8), cb:2 * cb], xp_ref[:, cb:2 * cb])
            edge = jnp.where(jnp.logical_and(j == 0, tblk == nt - 1), 0.0, 1.0)
            xs_r = jnp.where(rows == 0, pltpu.roll(prev_r, 1, 0) * edge, pltpu.roll(x_ref[pl.ds(i, 8), 0:cb], 1, 0))
            xs_i = jnp.where(rows == 0, pltpu.roll(prev_i, 1, 0) * edge, pltpu.roll(x_ref[pl.ds(i, 8), cb:2 * cb], 1, 0))
            sr_acc = sr_acc + lr * xs_r + li * xs_i
            si_acc = si_acc + li * xs_r - lr * xs_i
            return lr[0:1, :], li[0:1, :], sr_acc, si_acc

        cr, ci, sr_acc, si_acc = lax.fori_loop(
            0, nj, step, (carry[0:1, :], carry[1:2, :], acc[:, 0:cb], acc[:, cb:2 * cb]))
        carry[0:1, :] = cr
        carry[1:2, :] = ci
        acc[:, 0:cb] = sr_acc
        acc[:, cb:2 * cb] = si_acc

        @pl.when(tblk == nt - 1)
        def _():
            da_ref[...] = jnp.sum(acc[...], axis=0, keepdims=True)

    blk = pl.BlockSpec((tb, 2 * cb), lambda j, i: (nt - 1 - i, j))
    prev = pl.BlockSpec((8, 2 * cb), lambda j, i: (jnp.maximum((nt - 1 - i) * (tb // 8) - 1, 0), j))
    vec = pl.BlockSpec((1, 2 * cb), lambda j, i: (0, j))
    return _call(
        body, name=name, grid=(SSM_CH // cb, nt), in_specs=[blk, blk, prev, vec], out_specs=[blk, vec],
        out_shape=[jax.ShapeDtypeStruct((t, 2 * SSM_CH), F32), jax.ShapeDtypeStruct((1, 2 * SSM_CH), F32)],
        scratch_shapes=[pltpu.VMEM((8, cb), F32), pltpu.VMEM((8, 2 * cb), F32)],
        sem=("parallel", "arbitrary"), args=(dx, x, x, a), comm=comm)


def _glu_fwd(yc, u, dvec, wg, bg, name, tr=256):
    t = yc.shape[0]

    def body(yc_ref, u_ref, d_ref, w_ref, b_ref, yl_ref, yb_ref):
        yl = yc_ref[...] + d_ref[...] * u_ref[...]
        yl_ref[...] = yl
        yg, _ = _gelu_and_grad(yl)
        z = jnp.dot(yg.astype(BF16), w_ref[...], preferred_element_type=F32) + b_ref[...]
        yb_ref[...] = (yg * _sigmoid(z)).astype(BF16)

    blk = pl.BlockSpec((tr, SSM_WIDTH), lambda i: (i, 0))
    vec = pl.BlockSpec((1, SSM_WIDTH), lambda i: (0, 0))
    return pl.pallas_call(
        body, name=name, grid=(t // tr,),
        in_specs=[blk, blk, vec, pl.BlockSpec((SSM_WIDTH, SSM_WIDTH), lambda i: (0, 0)), vec],
        out_specs=[blk, blk],
        out_shape=[jax.ShapeDtypeStruct((t, SSM_WIDTH), F32), jax.ShapeDtypeStruct((t, SSM_WIDTH), BF16)],
        compiler_params=_params(("parallel",)))(yc, u, dvec, wg, bg)


def _glu_bwd(yl, u, dvec, wg, bg, dyb, name, tr=256):
    t = yl.shape[0]

    def body(yl_ref, u_ref, d_ref, w_ref, b_ref, dy_ref, dyl_ref, du_ref, dw_ref, db_ref, dd_ref):
        @pl.when(pl.program_id(0) == 0)
        def _():
            dw_ref[...] = jnp.zeros_like(dw_ref)
            db_ref[...] = jnp.zeros_like(db_ref)
            dd_ref[...] = jnp.zeros_like(dd_ref)

        ylv, dyv, wv = yl_ref[...], dy_ref[...].astype(F32), w_ref[...]
        yg, dgelu = _gelu_and_grad(ylv)
        ygb = yg.astype(BF16)
        z = jnp.dot(ygb, wv, preferred_element_type=F32) + b_ref[...]
        sg = _sigmoid(z)
        dz = dyv * yg * sg * (1.0 - sg)
        dzb = dz.astype(BF16)
        dyg = dyv * sg + lax.dot_general(dzb, wv, (((1,), (1,)), ((), ())), preferred_element_type=F32)
        dyl = dyg * dgelu
        dyl_ref[...] = dyl.astype(BF16)
        du_ref[...] = dyl * d_ref[...]
        dw_ref[...] += lax.dot_general(ygb, dzb, (((0,), (0,)), ((), ())), preferred_element_type=F32)
        db_ref[...] += jnp.sum(dz, axis=0, keepdims=True)
        dd_ref[...] += jnp.sum(dyl * u_ref[...], axis=0, keepdims=True)

    blk = pl.BlockSpec((tr, SSM_WIDTH), lambda i: (i, 0))
    vec = pl.BlockSpec((1, SSM_WIDTH), lambda i: (0, 0))
    wsp = pl.BlockSpec((SSM_WIDTH, SSM_WIDTH), lambda i: (0, 0))
    return pl.pallas_call(
        body, name=name, grid=(t // tr,), in_specs=[blk, blk, vec, wsp, vec, blk],
        out_specs=[blk, blk, wsp, vec, vec],
        out_shape=[jax.ShapeDtypeStruct((t, SSM_WIDTH), BF16), jax.ShapeDtypeStruct((t, SSM_WIDTH), F32),
                   jax.ShapeDtypeStruct((SSM_WIDTH, SSM_WIDTH), F32), jax.ShapeDtypeStruct((1, SSM_WIDTH), F32),
                   jax.ShapeDtypeStruct((1, SSM_WIDTH), F32)],
        compiler_params=_params(("arbitrary",)))(yl, u, dvec, wg, bg, dyb)


def _mesh_pos():
    return lax.axis_index("x"), lax.axis_index("y"), lax.axis_index("c")


def _device_index():
    x, y, c = _mesh_pos()
    return 4 * x + 2 * y + c


def _gather_comm(arrays):
    na = len(arrays)

    def own_copy(ins, outs, sems, ai):
        return pltpu.make_async_copy(ins[ai], outs[ai].at[_device_index()], sems[2].at[ai])

    def ctx(ins, outs, sems):
        send_sems, recv_sems = sems[:2]
        x, y, c = _mesh_pos()
        chips = [(1 - x, y), (x, 1 - y), (1 - x, 1 - y)]

        def copy(ai, kk, block, to, own=False):
            slot = outs[ai].at[4 * block[0] + 2 * block[1] + block[2]]
            return pltpu.make_async_remote_copy(
                src_ref=ins[ai] if own else slot, dst_ref=slot, send_sem=send_sems.at[ai, kk],
                recv_sem=recv_sems.at[ai, kk], device_id=to, device_id_type=MESH)

        return (x, y, c), (x, y, 1 - c), chips, c, copy

    def start(ins, outs, sems):
        me, sibling, chips, c, copy = ctx(ins, outs, sems)
        for ai in range(na):
            copy(ai, 0, me, sibling, own=True).start()
            for j, chip in enumerate(chips):
                copy(ai, 1 + j, me, (*chip, c), own=True).start()
        for ai in range(na):
            own_copy(ins, outs, sems, ai).start()

    def mid(ins, outs, sems):
        me, sibling, chips, c, copy = ctx(ins, outs, sems)
        for ai in range(na):
            for j, chip in enumerate(chips):
                copy(ai, 1 + j, (*chip, c), me).wait_recv()
                copy(ai, 4 + j, (*chip, c), sibling).start()

    def end(ins, outs, sems):
        me, sibling, chips, c, copy = ctx(ins, outs, sems)
        for ai in range(na):
            copy(ai, 0, sibling, me).wait_recv()
            copy(ai, 0, me, sibling, own=True).wait_send()
            for j, chip in enumerate(chips):
                copy(ai, 4 + j, (*chip, 1 - c), me).wait_recv()
                copy(ai, 1 + j, me, (*chip, c), own=True).wait_send()
                copy(ai, 4 + j, (*chip, c), sibling).wait_send()
            own_copy(ins, outs, sems, ai).wait()

    return Comm(arrays, [jax.ShapeDtypeStruct((N_DEV,) + a.shape, a.dtype) for a in arrays],
                [pltpu.SemaphoreType.DMA((na, 7)), pltpu.SemaphoreType.DMA((na, 7)), pltpu.SemaphoreType.DMA((na,))],
                start, end, mid)


def _sequencer_gather(arrays, name, collective_id):
    comm = _gather_comm(arrays)
    na = len(arrays)

    def body(*refs):
        ins, outs, sems = refs[:na], refs[na:2 * na], refs[2 * na:]
        x, y, c = _mesh_pos()
        peers = [(x, y, 1 - c), (1 - x, y, c), (x, 1 - y, c), (1 - x, 1 - y, c)]
        barrier = pltpu.get_barrier_semaphore()
        for peer in peers:
            pl.semaphore_signal(barrier, inc=1, device_id=peer, device_id_type=MESH)
        pl.semaphore_wait(barrier, len(peers))
        comm.start(ins, outs, sems)
        comm.mid(ins, outs, sems)
        comm.end(ins, outs, sems)

    return list(pl.kernel(
        body, out_type=tuple(comm.out_shapes), mesh=plsc.ScalarSubcoreMesh(axis_name="sequencer", num_cores=1),
        name=name, scratch_types=tuple(comm.sems),
        compiler_params=pltpu.CompilerParams(collective_id=collective_id))(*arrays))


def _sequencer_exchange(comm, peers_of, name, collective_id):
    na = len(comm.inputs)

    def body(*refs):
        ins, outs, sems = refs[:na], refs[na:na + len(comm.out_shapes)], refs[na + len(comm.out_shapes):]
        peers = peers_of(*_mesh_pos())
        barrier = pltpu.get_barrier_semaphore()
        for peer in peers:
            pl.semaphore_signal(barrier, inc=1, device_id=peer, device_id_type=MESH)
        pl.semaphore_wait(barrier, len(peers))
        comm.start(ins, outs, sems)
        comm.end(ins, outs, sems)

    return list(pl.kernel(
        body, out_type=tuple(comm.out_shapes), mesh=plsc.ScalarSubcoreMesh(axis_name="sequencer", num_cores=1),
        name=name, scratch_types=tuple(comm.sems),
        compiler_params=pltpu.CompilerParams(collective_id=collective_id))(*comm.inputs))


SIBLING_SWAP_ID, CHIP_EXCHANGE_ID = 9, 10


def _sequencer_swap(arrays, name):
    return _sequencer_exchange(_swap_comm(arrays), lambda x, y, c: [(x, y, 1 - c)], name, SIBLING_SWAP_ID)[0]


def _sequencer_chips(send, name):
    return _sequencer_exchange(_chips_comm(send), lambda x, y, c: [(1 - x, y, c), (x, 1 - y, c), (1 - x, 1 - y, c)],
                               name, CHIP_EXCHANGE_ID)[0]


def _swap_comm(arrays):
    na = len(arrays)
    offs = np.concatenate([[0], np.cumsum([a.shape[1] for a in arrays])]).astype(int)

    def copies(ins, outs, sems):
        x, y, c = _mesh_pos()
        return [pltpu.make_async_remote_copy(
            src_ref=ins[ai].at[2 * k + 1 - c], dst_ref=outs[0].at[k, pl.ds(int(offs[ai]), arrays[ai].shape[1])],
            send_sem=sems[0].at[ai, k], recv_sem=sems[1].at[ai, k], device_id=(x, y, 1 - c), device_id_type=MESH)
            for ai in range(na) for k in range(4)]

    def start(ins, outs, sems):
        for cp in copies(ins, outs, sems):
            cp.start()

    def end(ins, outs, sems):
        for cp in copies(ins, outs, sems):
            cp.wait()

    return Comm(arrays, [jax.ShapeDtypeStruct((4, int(offs[-1]), PACK_COLS), arrays[0].dtype)],
                [pltpu.SemaphoreType.DMA((na, 4)), pltpu.SemaphoreType.DMA((na, 4))], start, end)


def _chips_comm(send):
    def copies(ins, outs, sems):
        x, y, c = _mesh_pos()
        chips = [(1 - x, y), (x, 1 - y), (1 - x, 1 - y)]
        return [pltpu.make_async_remote_copy(
            src_ref=ins[0].at[2 * cx + cy], dst_ref=outs[0].at[j], send_sem=sems[0].at[j], recv_sem=sems[1].at[j],
            device_id=(cx, cy, c), device_id_type=MESH) for j, (cx, cy) in enumerate(chips)]

    def start(ins, outs, sems):
        for cp in copies(ins, outs, sems):
            cp.start()

    def end(ins, outs, sems):
        for cp in copies(ins, outs, sems):
            cp.wait()

    return Comm([send], [jax.ShapeDtypeStruct((3,) + send.shape[1:], send.dtype)],
                [pltpu.SemaphoreType.DMA((3,)), pltpu.SemaphoreType.DMA((3,))], start, end)


def _all_gather(arrays, name):
    na = len(arrays)

    def body(*refs):
        ins, outs = refs[:na], refs[na:2 * na]
        send_sems, recv_sems, local_sems = refs[2 * na:]
        x, y, c = _mesh_pos()
        me, sibling = (x, y, c), (x, y, 1 - c)
        chips = [(1 - x, y), (x, 1 - y), (1 - x, 1 - y)]
        waits = []
        for ai in range(na):
            in_ref, out_ref = ins[ai], outs[ai]

            def slot(px, py, pc, out_ref=out_ref):
                return out_ref.at[4 * px + 2 * py + pc]

            def copy(kk, block, to, src=None, ai=ai, slot=slot):
                return pltpu.make_async_remote_copy(
                    src_ref=slot(*block) if src is None else src, dst_ref=slot(*block),
                    send_sem=send_sems.at[ai, kk], recv_sem=recv_sems.at[ai, kk], device_id=to, device_id_type=MESH)

            mine = pltpu.make_async_copy(in_ref, slot(*me), local_sems.at[ai])
            mine.start()
            first = [copy(0, me, sibling, src=in_ref)]
            first += [copy(1 + j, me, (*chip, c), src=in_ref) for j, chip in enumerate(chips)]
            for cp in first:
                cp.start()
            waits.append((copy, mine, first))
        sends = []
        for ai in range(na):
            copy, mine, first = waits[ai]
            passed = [copy(4 + j, (*chip, c), sibling) for j, chip in enumerate(chips)]
            for j, chip in enumerate(chips):
                copy(1 + j, (*chip, c), me).wait_recv()
                passed[j].start()
            sends.append(passed)
        for ai in range(na):
            copy, mine, first = waits[ai]
            copy(0, sibling, me).wait_recv()
            for j, chip in enumerate(chips):
                copy(4 + j, (*chip, 1 - c), me).wait_recv()
            for cp in first + sends[ai]:
                cp.wait_send()
            mine.wait()

    any_spec = pl.BlockSpec(memory_space=pl.ANY)
    return pl.pallas_call(
        body, name=name, in_specs=[any_spec] * na, out_specs=[any_spec] * na,
        out_shape=[jax.ShapeDtypeStruct((N_DEV,) + a.shape, a.dtype) for a in arrays],
        scratch_shapes=[pltpu.SemaphoreType.DMA((na, 7)), pltpu.SemaphoreType.DMA((na, 7)),
                        pltpu.SemaphoreType.DMA((na,))],
        compiler_params=pltpu.CompilerParams(has_side_effects=True))(*arrays)


def _swap_sibling(arrays, name):
    na = len(arrays)
    offs = np.concatenate([[0], np.cumsum([a.shape[1] for a in arrays])]).astype(int)
    rows = int(offs[-1])

    def body(*refs):
        ins, recv_ref = refs[:na], refs[na]
        send_sems, recv_sems = refs[na + 1:]
        x, y, c = _mesh_pos()
        started = []
        for ai in range(na):
            span = pl.ds(int(offs[ai]), arrays[ai].shape[1])
            for k in range(4):
                remote = pltpu.make_async_remote_copy(
                    src_ref=ins[ai].at[2 * k + 1 - c], dst_ref=recv_ref.at[k, span], send_sem=send_sems.at[ai, k],
                    recv_sem=recv_sems.at[ai, k], device_id=(x, y, 1 - c), device_id_type=MESH)
                remote.start()
                started.append(remote)
        for remote in started:
            remote.wait()

    any_spec = pl.BlockSpec(memory_space=pl.ANY)
    return pl.pallas_call(
        body, name=name, in_specs=[any_spec] * na, out_specs=any_spec,
        out_shape=jax.ShapeDtypeStruct((4, rows, PACK_COLS), arrays[0].dtype),
        scratch_shapes=[pltpu.SemaphoreType.DMA((na, 4)), pltpu.SemaphoreType.DMA((na, 4))])(*arrays)


def _exchange_chips(send, name):
    def body(s_ref, o_ref, send_sems, recv_sems):
        x, y, c = _mesh_pos()
        chips = [(1 - x, y), (x, 1 - y), (1 - x, 1 - y)]
        cps = [pltpu.make_async_remote_copy(
            src_ref=s_ref.at[2 * cx + cy], dst_ref=o_ref.at[j], send_sem=send_sems.at[j], recv_sem=recv_sems.at[j],
            device_id=(cx, cy, c), device_id_type=MESH) for j, (cx, cy) in enumerate(chips)]
        for cp in cps:
            cp.start()
        for cp in cps:
            cp.wait()

    any_spec = pl.BlockSpec(memory_space=pl.ANY)
    return pl.pallas_call(
        body, name=name, in_specs=[any_spec], out_specs=any_spec,
        out_shape=jax.ShapeDtypeStruct((3,) + send.shape[1:], send.dtype),
        scratch_shapes=[pltpu.SemaphoreType.DMA((3,)), pltpu.SemaphoreType.DMA((3,))])(send)


def _pair_sum(keep, recv, name, tr=464):
    nchip, rows, cols = keep.shape

    def body(g_ref, r_ref, o_ref):
        o_ref[...] = (g_ref[...].astype(F32) + r_ref[...].astype(F32)).astype(BF16)

    blk = pl.BlockSpec((1, tr, cols), lambda k, i: (k, i, 0))
    return pl.pallas_call(
        body, name=name, grid=(nchip, rows // tr), in_specs=[blk, blk], out_specs=blk,
        out_shape=jax.ShapeDtypeStruct((nchip, rows, cols), BF16),
        compiler_params=_params(("parallel", "parallel")))(keep, recv)


def _pair_sum_pieces(pieces, recv, name, tr):
    _, rows, cols = pieces.shape
    core = lax.axis_index("c").astype(jnp.int32).reshape(1)

    def body(c_ref, g_ref, r_ref, o_ref):
        del c_ref
        o_ref[...] = (g_ref[...].astype(F32) + r_ref[...].astype(F32)).astype(BF16)

    grid_spec = pltpu.PrefetchScalarGridSpec(
        num_scalar_prefetch=1, grid=(4, rows // tr),
        in_specs=[pl.BlockSpec((1, tr, cols), lambda k, i, c_ref: (2 * k + c_ref[0], i, 0)),
                  pl.BlockSpec((1, tr, cols), lambda k, i, c_ref: (k, i, 0))],
        out_specs=pl.BlockSpec((1, tr, cols), lambda k, i, c_ref: (k, i, 0)))
    return pl.pallas_call(
        body, name=name, grid_spec=grid_spec, out_shape=jax.ShapeDtypeStruct((4, rows, cols), BF16),
        compiler_params=_params(("parallel", "parallel")))(core, pieces, recv)


def _chip_sum(own, others, name, tr=464):
    _, rows, cols = own.shape
    chip = (2 * lax.axis_index("x") + lax.axis_index("y")).astype(jnp.int32).reshape(1)

    def body(chip_ref, own_ref, oth_ref, o_ref):
        del chip_ref
        acc = own_ref[0].astype(F32)
        for j in range(3):
            acc = acc + oth_ref[j].astype(F32)
        o_ref[...] = acc

    grid_spec = pltpu.PrefetchScalarGridSpec(
        num_scalar_prefetch=1, grid=(rows // tr,),
        in_specs=[pl.BlockSpec((1, tr, cols), lambda i, chip_ref: (chip_ref[0], i, 0)),
                  pl.BlockSpec((3, tr, cols), lambda i, chip_ref: (0, i, 0))],
        out_specs=pl.BlockSpec((tr, cols), lambda i, chip_ref: (i, 0)))
    return pl.pallas_call(
        body, name=name, grid_spec=grid_spec, out_shape=jax.ShapeDtypeStruct((rows, cols), F32),
        compiler_params=_params(("parallel",)))(chip, own, others)


def _sum_leading(parts, name, tr=464):
    nparts, rows, cols = parts.shape
    tr = tr if rows % tr == 0 else rows

    def body(p_ref, o_ref):
        acc = p_ref[0].astype(F32)
        for i in range(1, nparts):
            acc = acc + p_ref[i].astype(F32)
        o_ref[...] = acc

    return pl.pallas_call(
        body, name=name, grid=(rows // tr,),
        in_specs=[pl.BlockSpec((nparts, tr, cols), lambda i: (0, i, 0))],
        out_specs=pl.BlockSpec((tr, cols), lambda i: (i, 0)), out_shape=jax.ShapeDtypeStruct((rows, cols), F32),
        compiler_params=_params(("parallel",)))(parts)


def _adamw(w, g, m, v, name, comm=None):
    shape = w.shape
    cols = shape[-1]
    lead = shape[0] if len(shape) >= 3 else 1
    rows = int(np.prod(shape[:-1])) // lead if len(shape) > 1 else 1
    w2, g2, m2, v2 = (a.reshape(lead, rows, cols) for a in (w, g, m, v))
    tr = rows
    for cand in (512, 256, 128, 64, 32, 16, 8):
        if rows % cand == 0 and rows > cand:
            tr = cand
            break
    bc1, bc2 = 1.0 - ADAM_B1 ** ADAM_STEP, 1.0 - ADAM_B2 ** ADAM_STEP

    def body(w_ref, g_ref, m_ref, v_ref, d_ref, nm_ref, nv_ref):
        gv = g_ref[...]
        nm = ADAM_B1 * m_ref[...] + (1.0 - ADAM_B1) * gv
        nv = ADAM_B2 * v_ref[...] + (1.0 - ADAM_B2) * (gv * gv)
        nm_ref[...] = nm
        nv_ref[...] = nv
        d_ref[...] = -ADAM_LR * ((nm / bc1) / (jnp.sqrt(nv / bc2) + ADAM_EPS) + ADAM_WD * w_ref[...])

    blk = pl.BlockSpec((1, tr, cols), lambda l, i: (l, i, 0))
    res = _call(body, name=name, grid=(lead, rows // tr), in_specs=[blk] * 4, out_specs=[blk] * 3,
                out_shape=[jax.ShapeDtypeStruct((lead, rows, cols), F32)] * 3, sem=("parallel", "parallel"),
                args=(w2, g2, m2, v2), comm=comm)
    outs, couts = res if comm is not None else (res, None)
    outs = tuple(o.reshape(shape) for o in outs)
    return outs if comm is None else (outs, couts)


WEIGHT_NAMES = ['norm_mix_g', 'norm_xa_g', 'norm_ffn_g', 'norm_mem_g', 'norm_final_g', 'w_in_ab', 'conv_qkv_a',
                'a_log_a', 'dt_bias_a', 'onorm_g_a', 'ssm_lambda_re', 'ssm_lambda_im', 'ssm_b_re', 'ssm_b_im',
                'ssm_c_re', 'ssm_c_im', 'ssm_d', 'ssm_log_dt', 'w_glu_b', 'b_glu_b', 'w_out_ab', 'pool_w',
                'pool_scale', 'xa_wq', 'xa_wkv', 'xa_wo', 'ffn_w_up', 'ffn_conv', 'ffn_w_down']
BIG_SHARDED = {'w_in_ab': ((1, 1024, 2568), 2), 'w_glu_b': ((1, 512, 512), 1), 'w_out_ab': ((1, 1024, 1024), 1),
               'pool_w': ((1, 4, 256, 256), 2), 'xa_wq': ((2, 1024, 1024), 1), 'xa_wkv': ((2, 1024, 2048), 2),
               'xa_wo': ((2, 1024, 1024), 1), 'ffn_w_up': ((2, 1024, 5632), 2), 'ffn_w_down': ((2, 2816, 1024), 1)}
SMALL_SHARDED = {'conv_qkv_a': ((1, 4, 1536), 2), 'pool_scale': ((1, 1024), 1), 'ffn_conv': ((2, 3, 5632), 2)}
REPLICATED = {'norm_mix_g': (2, 1024), 'norm_xa_g': (2, 1024), 'norm_ffn_g': (2, 1024), 'norm_mem_g': (1024,),
              'norm_final_g': (1024,), 'a_log_a': (1, 4), 'dt_bias_a': (1, 4), 'onorm_g_a': (1, 128),
              'ssm_lambda_re': (1, 32, 64), 'ssm_lambda_im': (1, 32, 64), 'ssm_b_re': (1, 32, 64, 16),
              'ssm_b_im': (1, 32, 64, 16), 'ssm_c_re': (1, 32, 16, 64), 'ssm_c_im': (1, 32, 16, 64),
              'ssm_d': (1, 32, 16), 'ssm_log_dt': (1, 32), 'b_glu_b': (1, 512)}
PACK_ROW_ALIGN = 8


def _shard_shape(shape, axis):
    return tuple(s // N_DEV if i == axis else s for i, s in enumerate(shape))


def _round_up(n, m):
    return (n + m - 1) // m * m


def _pack(arrays):
    total = sum(int(np.prod(a.shape)) for a in arrays)
    padded = _round_up(total, PACK_COLS * PACK_ROW_ALIGN)
    parts = [a.astype(F32).reshape(-1) for a in arrays]
    if padded != total:
        parts.append(jnp.zeros((padded - total,), F32))
    return jnp.concatenate(parts).reshape(padded // PACK_COLS, PACK_COLS)


def _unpack(packed, shapes):
    flat, out, off = packed.reshape(-1), [], 0
    for shape in shapes:
        size = int(np.prod(shape))
        out.append(flat[off:off + size].reshape(shape))
        off += size
    return out


def _split_shards(full, axis):
    shape = full.shape
    s = shape[axis] // N_DEV
    a = full.reshape(shape[:axis] + (N_DEV, s) + shape[axis + 1:])
    return jnp.moveaxis(a, axis, 0).reshape(N_DEV, -1)


def _merge_shards(pieces, shape, axis):
    sh = _shard_shape(shape, axis)
    a = pieces.reshape((N_DEV,) + sh)
    a = jnp.moveaxis(a, 0, axis)
    return a.reshape(shape)


_SCAN_NB = SSM_CH // SCAN_CB


def _to_scan_layout(m, axis):
    shape = m.shape
    m = m.reshape(shape[:axis] + (2, _SCAN_NB, SCAN_CB) + shape[axis + 1:])
    return jnp.swapaxes(m, axis, axis + 1).reshape(shape)


def _from_scan_layout(m, axis):
    shape = m.shape
    m = m.reshape(shape[:axis] + (_SCAN_NB, 2, SCAN_CB) + shape[axis + 1:])
    return jnp.swapaxes(m, axis, axis + 1).reshape(shape)


def _s5_discretise(lam_re, lam_im, b_re, b_im, log_dt):
    dt = jnp.exp(log_dt)[:, None]
    mag = jnp.exp(lam_re * dt)
    ang = lam_im * dt
    lb_re, lb_im = mag * jnp.cos(ang), mag * jnp.sin(ang)
    den = lam_re * lam_re + lam_im * lam_im
    nr, ni = lb_re - 1.0, lb_im
    coef_re = (nr * lam_re + ni * lam_im) / den
    coef_im = (ni * lam_re - nr * lam_im) / den
    bb_re = coef_re[..., None] * b_re - coef_im[..., None] * b_im
    bb_im = coef_re[..., None] * b_im + coef_im[..., None] * b_re
    return lb_re, lb_im, bb_re, bb_im


_GROUPS_PER_BLOCK = N_GROUPS // _SCAN_NB
_U_BLOCK = _GROUPS_PER_BLOCK * SSM_GROUP


def _s5_matrices(lb_re, lb_im, bb_re, bb_im, c_re, c_im):
    eye = jnp.eye(_GROUPS_PER_BLOCK, dtype=F32)
    blocked = lambda m: m.reshape((_SCAN_NB, _GROUPS_PER_BLOCK) + m.shape[1:])
    bmat = lambda bb: jnp.einsum('jgph,gk->jghkp', blocked(bb), eye).reshape(_SCAN_NB, _U_BLOCK, SCAN_CB)
    cmat = lambda cc: jnp.einsum('jghp,gk->jkpgh', blocked(cc), eye).reshape(_SCAN_NB, SCAN_CB, _U_BLOCK)
    b_in = jnp.concatenate([bmat(bb_re), bmat(bb_im)], axis=2)
    c_out = jnp.concatenate([cmat(c_re), -cmat(c_im)], axis=1)
    a_row = _to_scan_layout(jnp.concatenate([lb_re.reshape(1, SSM_CH), lb_im.reshape(1, SSM_CH)], axis=1), 1)
    return b_in, c_out, a_row


def _s5_matrix_grads(db_in, dc_out, da_row):
    da_nat = _from_scan_layout(da_row, 1)
    eye = jnp.eye(_GROUPS_PER_BLOCK, dtype=F32)
    nb, gb = _SCAN_NB, _GROUPS_PER_BLOCK
    bgrad = lambda m: jnp.einsum('jghkp,gk->jgph', m.reshape(nb, gb, SSM_GROUP, gb, SSM_STATE), eye
                                 ).reshape(N_GROUPS, SSM_STATE, SSM_GROUP)
    cgrad = lambda m: jnp.einsum('jkpgh,gk->jghp', m.reshape(nb, gb, SSM_STATE, gb, SSM_GROUP), eye
                                 ).reshape(N_GROUPS, SSM_GROUP, SSM_STATE)
    dbb_re, dbb_im = bgrad(db_in[:, :, :SCAN_CB]), bgrad(db_in[:, :, SCAN_CB:])
    dc_re, dc_im = cgrad(dc_out[:, :SCAN_CB]), -cgrad(dc_out[:, SCAN_CB:])
    dlb_re = da_nat[0, :SSM_CH].reshape(N_GROUPS, SSM_STATE)
    dlb_im = da_nat[0, SSM_CH:].reshape(N_GROUPS, SSM_STATE)
    return dlb_re, dlb_im, dbb_re, dbb_im, dc_re, dc_im


def _as_pieces(a):
    return a.reshape(N_DEV, a.shape[0] // N_DEV, a.shape[1])


def _hybrid_fwd(xn, x, wts, p, weights, riders):
    sv = {}
    hq = _mm(xn, wts['w_qkv_t'], "nt", "l0_in_qkv")
    gate = _mm(xn, wts['w_gate_t'], "nt", "l0_in_gate")
    ba = _mm(xn, wts['w_ba_t'], "nt", "l0_in_ba")
    u = _mm(xn, wts['w_u_t'], "nt", "l0_in_u")
    conv = p['conv_qkv']
    q = _qkv_pre_fwd(hq, conv, 0, 4, True, HEAD_A ** -0.5, "l0_q_pre")
    k = _qkv_pre_fwd(hq, conv, 4, 4, True, 1.0, "l0_k_pre")
    v = _qkv_pre_fwd(hq, conv, 8, 4, False, 1.0, "l0_v_pre")
    gates = _gates_fwd(ba, p['arow'], p['brow'], "l0_gates")
    o, tm_all, s_all = riders.run("l0_gdr_fwd", _gdr_fwd, q, k, v, gates)
    wts['w_glu'], wts['w_out'] = weights.full['w_glu'], weights.full['w_out']
    y_a = _onorm_fwd(o, gate, p['onorm_g'], "l0_onorm")
    bu = riders.run("l0_s5_bu", _mm_bd, u, p['b_in'], "nn")
    xs = riders.run("l0_s5_scan", _s5_scan_fwd, bu, p['a_row'])
    weights.gather_by_sequencer(GATHER_LAYER1, xs, "gather_layer1", GATHER_LAYER1_ID)
    yc = riders.run("l0_s5_cx", _mm_bd, xs, p['c_out'], "nn")
    yl, y_b = _glu_fwd(yc, u, p['d_row'], wts['w_glu'], p['b_glu'], "l0_glu")
    mixed = jnp.concatenate([y_a, y_b], axis=1)
    x1 = _mm(mixed, wts['w_out'], "nn", "l0_out", res=x)
    sv.update(hq=hq, gate=gate, ba=ba, u=u, q=q, k=k, v=v, gb=gates, o=o, tm=tm_all, s=s_all, xs=xs, yl=yl, mixed=mixed)
    return x1, sv


def _hybrid_bwd(dx1, xn, wts, p, sv, riders):
    gr = {}
    dmixed = _mm(dx1, wts['w_out'], "nt", "l0_out_dx", out_dtype=BF16)
    riders.grad('w_out', _as_pieces(_mm(sv['mixed'], dx1, "tn", "l0_out_dw", out_dtype=BF16)))
    dya, dyb = dmixed[:, :WIDTH_A], dmixed[:, WIDTH_A:]
    dyl, du_direct, dw_glu, gr['b_glu_b'], dd = _glu_bwd(
        sv['yl'], sv['u'], p['d_row'], wts['w_glu'], p['b_glu'], dyb, "l0_glu_bwd")
    riders.grad('w_glu', dw_glu.astype(BF16).reshape(N_DEV, -1, PACK_COLS))
    dxs = riders.run("l0_s5_cx_dx", _mm_bd, dyl, p['c_out'], "nt")
    dc_out = _mm_bd(sv['xs'], dyl, "tn", "l0_s5_cx_dw")
    lam, da_row = riders.run("l0_s5_scan_bwd", _s5_scan_bwd, dxs, sv['xs'], p['a_row'])
    du = _mm_bd(lam, p['b_in'], "nt", "l0_s5_bu_dx", res=du_direct, out_dtype=BF16)
    db_in = _mm_bd(sv['u'], lam, "tn", "l0_s5_bu_dw")
    gr['s5'] = (db_in, dc_out, da_row, dd)
    do, dgate, gr['onorm_g_a'] = _onorm_bwd(sv['o'], sv['gate'], p['onorm_g'], dya, "l0_onorm_bwd")
    dq, dk, dv, dgb = riders.run("l0_gdr_bwd", _gdr_bwd, sv['q'], sv['k'], sv['v'], sv['gb'], sv['tm'], sv['s'], do)
    conv = p['conv_qkv']
    dhq_q, dcw_q = _qkv_pre_bwd(sv['hq'], conv, dq, 0, 4, True, HEAD_A ** -0.5, "l0_q_pre_bwd")
    dhq_k, dcw_k = _qkv_pre_bwd(sv['hq'], conv, dk, 4, 4, True, 1.0, "l0_k_pre_bwd")
    dhq_v, dcw_v = _qkv_pre_bwd(sv['hq'], conv, dv, 8, 4, False, 1.0, "l0_v_pre_bwd")
    gr['conv_qkv_a'] = jnp.concatenate([dcw_q, dcw_k, dcw_v], axis=1)
    dhq = jnp.concatenate([dhq_q, dhq_k, dhq_v], axis=1)
    dba, da_log, ddt_bias = _gates_bwd(sv['ba'], p['arow'], p['brow'], dgb, "l0_gates_bwd")
    gr['a_log_a'], gr['dt_bias_a'] = da_log[:, 4:8], ddt_bias[:, 4:8]
    dw_qkv_t = _mm(dhq, xn, "tn", "l0_in_qkv_dw", out_dtype=BF16)
    dw_gate_t = _mm(dgate, xn, "tn", "l0_in_gate_dw", out_dtype=BF16)
    dw_ba_t = _mm(dba, xn, "tn", "l0_in_ba_dw", out_dtype=BF16)
    dw_u_t = _mm(du, xn, "tn", "l0_in_u_dw", out_dtype=BF16)
    dw_in_t = _as_pieces(jnp.concatenate([dw_qkv_t, dw_gate_t, dw_ba_t[:8], dw_u_t], axis=0))
    riders.grad('w_in_t', jnp.concatenate(
        [dw_in_t, jnp.zeros((N_DEV, dict(PIECES)['w_in_t'] - W_IN_PIECE, D_MODEL), BF16)], axis=1))
    dxn = riders.run("l0_in_qkv_dx", _mm, dhq, wts['w_qkv_t'], "nn")
    dxn = _mm(dgate, wts['w_gate_t'], "nn", "l0_in_gate_dx", res=dxn)
    dxn = _mm(dba, wts['w_ba_t'], "nn", "l0_in_ba_dx", res=dxn)
    dxn = riders.run("l0_in_u_dx", _mm, du, wts['w_u_t'], "nn", res=dxn)
    return dxn, gr


def _xa_fwd(x1, g, mem_n, wq, wkv_t, wo, tag, riders):
    xq = _rms_fwd(x1, g, BF16, tag + "_norm")
    q = _mm(xq, wq, "nn", tag + "_q", out_dtype=BF16)
    kv = _mm(mem_n, wkv_t, "nt", tag + "_kv", out_dtype=BF16)
    o = riders.run(tag + "_attn", _attn_fwd, q, kv)
    x2 = _mm(o, wo, "nn", tag + "_o", res=x1)
    return x2, dict(xq=xq, q=q, kv=kv, o=o)


def _xa_bwd(dx2, x1, g, mem_n, wq, wkv_t, wo, sv, tag, layer, riders):
    do = _mm(dx2, wo, "nt", tag + "_o_dx", out_dtype=BF16)
    riders.grad('wo%d' % layer, _as_pieces(_mm(sv['o'], dx2, "tn", tag + "_o_dw", out_dtype=BF16)))
    dq, dk, dv = _attn_bwd(sv['q'], sv['kv'], do, tag + "_attn_bwd")
    dkv = jnp.concatenate([dk, dv], axis=1).astype(BF16)
    dxq = _mm(dq, wq, "nt", tag + "_q_dx")
    riders.grad('wq%d' % layer, _as_pieces(_mm(sv['xq'], dq, "tn", tag + "_q_dw", out_dtype=BF16)))
    dmem_n = _mm(dkv, wkv_t, "nn", tag + "_kv_dx")
    riders.grad('wkv_t%d' % layer, _as_pieces(_mm(dkv, mem_n, "tn", tag + "_kv_dw", out_dtype=BF16)))
    dx1, dg = riders.run(tag + "_norm_bwd", _rms_bwd, x1, g, dxq, dx2)
    return dx1, dmem_n, dg


def _ffn_fwd(x2, g, w_up_t, conv, w_down, tag, riders):
    xf = _rms_fwd(x2, g, BF16, tag + "_norm")
    h = riders.run(tag + "_up", _mm, xf, w_up_t, "nt")
    a = riders.run(tag + "_act", _ffn_act_fwd, h, conv)
    x3 = _mm(a, w_down, "nn", tag + "_down", res=x2)
    return x3, dict(xf=xf, h=h, a=a)


def _ffn_bwd(dx3, x2, g, w_up_t, conv, w_down, sv, tag, layer, riders):
    da = _mm(dx3, w_down, "nt", tag + "_down_dx")
    riders.grad('down%d' % layer, _as_pieces(_mm(sv['a'], dx3, "tn", tag + "_down_dw", out_dtype=BF16)))
    dh, dconv = riders.run(tag + "_act_bwd", _ffn_act_bwd, sv['h'], conv, da)
    dxf = riders.run(tag + "_up_dx", _mm, dh, w_up_t, "nn")
    dw_up_t = riders.run(tag + "_up_dw", _mm, dh, sv['xf'], "tn", out_dtype=BF16)
    riders.grad('up_t%d' % layer, _as_pieces(dw_up_t))
    dx2, dg = riders.run(tag + "_norm_bwd", _rms_bwd, x2, g, dxf, dx3)
    return dx2, dconv, dg


BIG_NAMES, SMALL_NAMES, REP_NAMES = list(BIG_SHARDED), list(SMALL_SHARDED), list(REPLICATED)
BIG_SIZES = [int(np.prod(_shard_shape(*BIG_SHARDED[n]))) for n in BIG_NAMES]
SMALL_SIZES = [int(np.prod(_shard_shape(*SMALL_SHARDED[n]))) for n in SMALL_NAMES]


PIECES = [('w_in_t', 384), ('w_glu', 32), ('w_out', 128), ('pool_w', 32), ('wq0', 128), ('wq1', 128),
          ('wkv_t0', 256), ('wkv_t1', 256), ('wo0', 128), ('wo1', 128), ('up_t0', 704), ('up_t1', 704),
          ('down0', 352), ('down1', 352)]
PIECE_OFFS = dict(zip([k for k, _ in PIECES], np.concatenate([[0], np.cumsum([r for _, r in PIECES])[:-1]]).tolist()))
W_IN_ROWS = 4 * WIDTH_A + 2 * N_HEADS_A + SSM_WIDTH
W_IN_PIECE = W_IN_ROWS // N_DEV


def _row_tile(rows):
    return max(t for t in range(16, min(rows, 512) + 1, 16) if rows % t == 0)


class _Riders:
    def __init__(self):
        self.waiting = {}
        self.deferred = {}
        self.grads = {}
        self.groups = []
        self.reduced = {}

    def add(self, host, comm, then):
        self.waiting.setdefault(host, []).append((comm, then))

    def after(self, marker, then):
        self.deferred.setdefault(marker, []).append(then)

    def mark(self, name):
        for then in self.deferred.pop(name, []):
            then()

    def run(self, name, fn, *args, **kw):
        riders = self.waiting.pop(name, [])
        if not riders:
            out = fn(*args, name=name, **kw)
        else:
            out, couts = fn(*args, name=name, comm=[c for c, _ in riders], **kw)
            for (_, then), got in zip(riders, couts):
                then(got)
        self.mark(name)
        return out

    def grad(self, key, pieces):
        self.grads[key] = pieces
        for group in [g for g in self.groups if all(k in self.grads for k in g[1])]:
            self.groups.remove(group)
            self._reduce(*group)

    def _reduce(self, name, keys, pair_marker, sum_marker):
        arrays = [self.grads[k] for k in keys]
        rows = sum(a.shape[1] for a in arrays)
        tile = _row_tile(rows)
        from_sibling = _sequencer_swap(arrays, name + "_to_sibling")

        def after_swap():
            if len(arrays) == 1:
                chip_sums = _pair_sum_pieces(arrays[0], from_sibling, name + "_pair_sum", tr=tile)
            else:
                core = lax.axis_index("c")
                keep = jnp.concatenate(
                    [lax.dynamic_index_in_dim(a.reshape(4, 2, a.shape[1], PACK_COLS), core, 1, keepdims=False)
                     for a in arrays], axis=1)
                chip_sums = _pair_sum(keep, from_sibling, name + "_pair_sum", tr=tile)
            from_chips = _sequencer_chips(chip_sums, name + "_to_chips")

            def after_chips():
                total = _chip_sum(chip_sums, from_chips, name + "_chip_sum", tr=tile)
                off = 0
                for k, a in zip(keys, arrays):
                    self.reduced[k] = total[off:off + a.shape[1]]
                    off += a.shape[1]

            self.after(sum_marker, after_chips)

        self.after(pair_marker, after_swap)


class _Weights:
    def __init__(self, inp):
        bf = lambda a: a.astype(BF16)
        local = {'w_in_t': bf(inp['w_in_ab'][0]).T, 'w_glu': bf(inp['w_glu_b'][0]), 'w_out': bf(inp['w_out_ab'][0]),
                 'pool_w': bf(inp['pool_w'][0]),
                 'small': _pack([inp[n] for n in SMALL_NAMES])}
        for l in range(2):
            local['wq%d' % l] = bf(inp['xa_wq'][l])
            local['wkv_t%d' % l] = bf(inp['xa_wkv'][l]).T
            local['wo%d' % l] = bf(inp['xa_wo'][l])
            local['up_t%d' % l] = bf(inp['ffn_w_up'][l]).T
            local['down%d' % l] = bf(inp['ffn_w_down'][l])
        self.local, self.full = local, {}

    def plan(self, keys):
        return _gather_comm([self.local[k] for k in keys])

    def gather_by_sequencer(self, keys, after, name, collective_id):
        arrays = [self.local[k] for k in keys]
        tie = (after.reshape(-1)[0] * 0.0).astype(arrays[0].dtype)
        arrays[0] = arrays[0] + tie
        self.land(keys, _sequencer_gather(arrays, name, collective_id))

    def land(self, keys, gathered):
        for k, g in zip(keys, gathered):
            if k == 'small':
                off = 0
                for n, size in zip(SMALL_NAMES, SMALL_SIZES):
                    self.full[n] = _merge_shards(g.reshape(N_DEV, -1)[:, off:off + size], *SMALL_SHARDED[n])
                    off += size
            elif k == 'pool_w':
                self.full[k] = jnp.swapaxes(g, 0, 1).reshape(len(POOL_WINDOWS), POOL_GROUP, POOL_GROUP)
            else:
                self.full[k] = g.reshape(N_DEV * g.shape[1], g.shape[2])


GATHER_FIRST = ['w_in_t', 'small']
GATHER_RIDES = []
GATHER_LAYER0 = ['w_glu', 'w_out', 'wq0', 'wkv_t0', 'wo0', 'down0', 'up_t0']
GATHER_LAYER1 = ['pool_w', 'wq1', 'wkv_t1', 'wo1', 'up_t1', 'down1']
GATHER_LAYER0_ID, GATHER_LAYER1_ID = 7, 8
GRAD_RIDES = [('g_down1', ['down1'], 'l1_ffn_act_bwd', 'l1_xa_norm_bwd'),
              ('g_up1', ['up_t1'], 'l1_xa_norm_bwd', 'l0_ffn_up_dx'),
              ('g_xa1', ['wq1', 'wkv_t1', 'wo1', 'pool_w'], 'l0_ffn_act_bwd', 'l0_xa_norm_bwd'),
              ('g_down0', ['down0'], 'l0_ffn_up_dx', 'l0_s5_scan_bwd'),
              ('g_l0', ['up_t0', 'wq0', 'wkv_t0', 'wo0'], 'l0_s5_cx_dx', 'l0_in_qkv_dx'),
              ('g_out', ['w_out', 'w_glu'], 'l0_s5_scan_bwd', 'l0_in_qkv_dx'),
              ('g_in', ['w_in_t'], 'l0_in_u_dx', 'adamw_pool_w')]


def _local_step(inp):
    f32_of = lambda n: inp[n].astype(F32)
    weights = _Weights(inp)
    riders = _Riders()
    riders.groups = list(GRAD_RIDES)
    full = weights.full
    weights.land(GATHER_FIRST, _comm_only(weights.plan(GATHER_FIRST), "gather_first"))
    weights.gather_by_sequencer(GATHER_LAYER0, full['w_in_t'], "gather_layer0", GATHER_LAYER0_ID)
    for host, keys in GATHER_RIDES:
        riders.add(host, weights.plan(keys), functools.partial(weights.land, keys))
    w_in_t = full['w_in_t']
    wts0 = dict(w_qkv_t=w_in_t[:3 * WIDTH_A], w_gate_t=w_in_t[3 * WIDTH_A:4 * WIDTH_A],
                w_ba_t=jnp.concatenate([w_in_t[4 * WIDTH_A:4 * WIDTH_A + 8], jnp.zeros((LANE - 8, D_MODEL), BF16)], 0),
                w_u_t=w_in_t[4 * WIDTH_A + 8:])
    lb_disc, disc_vjp = jax.vjp(_s5_discretise, f32_of('ssm_lambda_re')[0], f32_of('ssm_lambda_im')[0],
                                f32_of('ssm_b_re')[0], f32_of('ssm_b_im')[0], f32_of('ssm_log_dt')[0])
    b_in, c_out, a_row = _s5_matrices(*lb_disc, f32_of('ssm_c_re')[0], f32_of('ssm_c_im')[0])
    zeros4 = jnp.zeros((1, 4), F32)
    p0 = dict(conv_qkv=full['conv_qkv_a'][0], onorm_g=f32_of('onorm_g_a'),
              arow=jnp.concatenate([zeros4, f32_of('a_log_a'), jnp.zeros((1, LANE - 8), F32)], 1),
              brow=jnp.concatenate([zeros4, f32_of('dt_bias_a'), jnp.zeros((1, LANE - 8), F32)], 1),
              b_in=b_in.astype(BF16), c_out=c_out.astype(BF16), a_row=a_row,
              d_row=f32_of('ssm_d').reshape(1, SSM_WIDTH), b_glu=f32_of('b_glu_b'))

    x0 = inp['x'][0]
    mem_n = _rms_fwd(inp['mem'][0], inp['norm_mem_g'], BF16, "mem_norm")
    xn0 = _rms_fwd(x0, inp['norm_mix_g'][0], BF16, "l0_mix_norm")
    x1, sv_mix0 = _hybrid_fwd(xn0, x0, wts0, p0, weights, riders)
    x2, sv_xa0 = _xa_fwd(x1, inp['norm_xa_g'][0], mem_n, full['wq0'], full['wkv_t0'], full['wo0'], "l0_xa", riders)
    x3, sv_ffn0 = _ffn_fwd(x2, inp['norm_ffn_g'][0], full['up_t0'], full['ffn_conv'][0], full['down0'], "l0_ffn", riders)
    xn1 = _rms_fwd(x3, inp['norm_mix_g'][1], F32, "l1_mix_norm")
    x4 = _pool_fwd(xn1, full['pool_w'], full['pool_scale'], x3, "l1_pool")
    x5, sv_xa1 = _xa_fwd(x4, inp['norm_xa_g'][1], mem_n, full['wq1'], full['wkv_t1'], full['wo1'], "l1_xa", riders)
    x6, sv_ffn1 = _ffn_fwd(x5, inp['norm_ffn_g'][1], full['up_t1'], full['ffn_conv'][1], full['down1'], "l1_ffn", riders)
    loss_part, dx6, dg_final = _loss_head(x6, inp['norm_final_g'], inp['loss_target'][0], "loss_head")

    dx5, dconv1, dg_ffn1 = _ffn_bwd(dx6, x5, inp['norm_ffn_g'][1], full['up_t1'], full['ffn_conv'][1], full['down1'],
                                    sv_ffn1, "l1_ffn", 1, riders)
    dx4, dmem1, dg_xa1 = _xa_bwd(dx5, x4, inp['norm_xa_g'][1], mem_n, full['wq1'], full['wkv_t1'], full['wo1'],
                                 sv_xa1, "l1_xa", 1, riders)
    dxn1, dpool_w, dpool_scale = _pool_bwd(xn1, full['pool_w'], full['pool_scale'], dx4, "l1_pool_bwd")
    pool_pieces = jnp.swapaxes(dpool_w.astype(BF16).reshape(len(POOL_WINDOWS), N_DEV, -1, POOL_GROUP), 0, 1)
    riders.grad('pool_w', pool_pieces.reshape(N_DEV, -1, PACK_COLS))
    dx3, dg_mix1 = riders.run("l1_mix_norm_bwd", _rms_bwd, x3, inp['norm_mix_g'][1], dxn1, dx4)
    dx2, dconv0, dg_ffn0 = _ffn_bwd(dx3, x2, inp['norm_ffn_g'][0], full['up_t0'], full['ffn_conv'][0], full['down0'],
                                    sv_ffn0, "l0_ffn", 0, riders)
    dx1, dmem0, dg_xa0 = _xa_bwd(dx2, x1, inp['norm_xa_g'][0], mem_n, full['wq0'], full['wkv_t0'], full['wo0'],
                                 sv_xa0, "l0_xa", 0, riders)
    dxn0, g_mix0 = _hybrid_bwd(dx1, xn0, wts0, p0, sv_mix0, riders)
    grad_x, dg_mix0 = _rms_bwd(x0, inp['norm_mix_g'][0], dxn0, dx1, "l0_mix_norm_bwd")
    _, dg_mem = _rms_bwd(inp['mem'][0], inp['norm_mem_g'], dmem0 + dmem1, None, "mem_norm_bwd")
    assert not riders.groups and not riders.waiting and all(k.startswith("adamw_") for k in riders.deferred), (
        riders.groups, list(riders.waiting), list(riders.deferred))

    db_in, dc_out, da_row, dd = g_mix0['s5']
    dlb_re, dlb_im, dbb_re, dbb_im, dc_re, dc_im = _s5_matrix_grads(db_in, dc_out, da_row)
    dlam_re, dlam_im, dbr, dbi, dlog_dt = disc_vjp((dlb_re, dlb_im, dbb_re, dbb_im))

    rep_grads = {
        'norm_mix_g': jnp.concatenate([dg_mix0, dg_mix1], 0), 'norm_xa_g': jnp.concatenate([dg_xa0, dg_xa1], 0),
        'norm_ffn_g': jnp.concatenate([dg_ffn0, dg_ffn1], 0), 'norm_mem_g': dg_mem.reshape(-1),
        'norm_final_g': dg_final.reshape(-1), 'a_log_a': g_mix0['a_log_a'], 'dt_bias_a': g_mix0['dt_bias_a'],
        'onorm_g_a': g_mix0['onorm_g_a'], 'ssm_lambda_re': dlam_re[None], 'ssm_lambda_im': dlam_im[None],
        'ssm_b_re': dbr[None], 'ssm_b_im': dbi[None], 'ssm_c_re': dc_re[None], 'ssm_c_im': dc_im[None],
        'ssm_d': dd.reshape(1, N_GROUPS, SSM_GROUP), 'ssm_log_dt': dlog_dt[None], 'b_glu_b': g_mix0['b_glu_b']}
    small_grads = {'conv_qkv_a': g_mix0['conv_qkv_a'][None], 'pool_scale': dpool_scale,
                   'ffn_conv': jnp.stack([dconv0, dconv1])}
    return loss_part, grad_x, riders, rep_grads, small_grads


ADAMW_ORDER = ['ffn_w_up', 'ffn_w_down', 'xa_wkv', 'xa_wq', 'xa_wo', 'w_out_ab', 'w_glu_b', 'pool_w', 'w_in_ab']


def _update(inp, loss_part, grad_x, riders, rep_grads, small_grads):
    dev = _device_index()
    misc_local = _pack([rep_grads[n] for n in REP_NAMES] + [small_grads[n] for n in SMALL_NAMES] + [loss_part])
    (misc_all,) = _sequencer_gather([misc_local], "gather_small_grads", GATHER_LAYER0_ID)
    piece = lambda key: riders.reduced[key]
    both = lambda name: jnp.stack([piece(name + '0'), piece(name + '1')])
    swap = lambda a: jnp.swapaxes(a, -1, -2)
    reduced = {'w_in_ab': lambda: piece('w_in_t')[:W_IN_PIECE][None],
               'w_glu_b': lambda: piece('w_glu').reshape(inp['w_glu_b'].shape),
               'w_out_ab': lambda: piece('w_out')[None], 'pool_w': lambda: piece('pool_w').reshape(inp['pool_w'].shape),
               'xa_wq': lambda: both('wq'), 'xa_wkv': lambda: both('wkv_t'), 'xa_wo': lambda: both('wo'),
               'ffn_w_up': lambda: both('up_t'), 'ffn_w_down': lambda: both('down')}
    transposed = ('w_in_ab', 'xa_wkv', 'ffn_w_up')
    grads, upd = {}, {}
    assert sorted(ADAMW_ORDER) == sorted(BIG_NAMES)
    for n in ADAMW_ORDER:
        fix = swap if n in transposed else (lambda a: a)
        g = reduced[n]()
        out = riders.run("adamw_" + n, _adamw, fix(inp[n]), g, fix(inp['m_' + n]), fix(inp['v_' + n]))
        upd[n], grads[n] = tuple(fix(o) for o in out), fix(g)
    assert not riders.waiting and not riders.deferred, (list(riders.waiting), list(riders.deferred))
    misc_sum = _sum_leading(misc_all, "small_grads_sum")
    misc = _unpack(misc_sum, [inp[n].shape for n in REP_NAMES] + [SMALL_SHARDED[n][0] for n in SMALL_NAMES] + [()])
    loss = misc.pop()
    for n, g in zip(REP_NAMES, misc):
        grads[n] = g
    for n, g in zip(SMALL_NAMES, misc[len(REP_NAMES):]):
        grads[n] = lax.dynamic_index_in_dim(_split_shards(g, SMALL_SHARDED[n][1]), dev, 0, keepdims=False
                                            ).reshape(inp[n].shape)
    tiny_names = REP_NAMES + SMALL_NAMES
    rep_total = sum(int(np.prod(inp[n].shape)) for n in REP_NAMES)
    packs = [_pack([inp[prefix + n] for n in tiny_names]) for prefix in ('', 'm_', 'v_')]
    g_pack = _pack([misc_sum.reshape(-1)[:rep_total]] + [grads[n] for n in SMALL_NAMES])
    tiny_out = [_unpack(o, [inp[n].shape for n in tiny_names])
                for o in _adamw(packs[0], g_pack, packs[1], packs[2], "adamw_small")]
    for i, n in enumerate(tiny_names):
        upd[n] = tuple(o[i] for o in tiny_out)

    outs = [loss, grad_x[None]]
    outs += [grads[n] for n in WEIGHT_NAMES]
    for i in range(3):
        outs += [upd[n][i] for n in WEIGHT_NAMES]
    return tuple(outs)


def _step(inp):
    loss_part, grad_x, riders, rep_grads, small_grads = _local_step(inp)
    return _update(inp, loss_part, grad_x, riders, rep_grads, small_grads)


INPUT_NAMES = (['x', 'mem'] + WEIGHT_NAMES + ['loss_target'] + ['m_' + n for n in WEIGHT_NAMES]
               + ['v_' + n for n in WEIGHT_NAMES])


def kernel(x, mem, norm_mix_g, norm_xa_g, norm_ffn_g, norm_mem_g, norm_final_g, w_in_ab, conv_qkv_a, a_log_a, dt_bias_a, onorm_g_a, ssm_lambda_re, ssm_lambda_im, ssm_b_re, ssm_b_im, ssm_c_re, ssm_c_im, ssm_d, ssm_log_dt, w_glu_b, b_glu_b, w_out_ab, pool_w, pool_scale, xa_wq, xa_wkv, xa_wo, ffn_w_up, ffn_conv, ffn_w_down, loss_target, m_norm_mix_g, m_norm_xa_g, m_norm_ffn_g, m_norm_mem_g, m_norm_final_g, m_w_in_ab, m_conv_qkv_a, m_a_log_a, m_dt_bias_a, m_onorm_g_a, m_ssm_lambda_re, m_ssm_lambda_im, m_ssm_b_re, m_ssm_b_im, m_ssm_c_re, m_ssm_c_im, m_ssm_d, m_ssm_log_dt, m_w_glu_b, m_b_glu_b, m_w_out_ab, m_pool_w, m_pool_scale, m_xa_wq, m_xa_wkv, m_xa_wo, m_ffn_w_up, m_ffn_conv, m_ffn_w_down, v_norm_mix_g, v_norm_xa_g, v_norm_ffn_g, v_norm_mem_g, v_norm_final_g, v_w_in_ab, v_conv_qkv_a, v_a_log_a, v_dt_bias_a, v_onorm_g_a, v_ssm_lambda_re, v_ssm_lambda_im, v_ssm_b_re, v_ssm_b_im, v_ssm_c_re, v_ssm_c_im, v_ssm_d, v_ssm_log_dt, v_w_glu_b, v_b_glu_b, v_w_out_ab, v_pool_w, v_pool_scale, v_xa_wq, v_xa_wkv, v_xa_wo, v_ffn_w_up, v_ffn_conv, v_ffn_w_down):
    args = (x, mem, norm_mix_g, norm_xa_g, norm_ffn_g, norm_mem_g, norm_final_g, w_in_ab, conv_qkv_a, a_log_a, dt_bias_a, onorm_g_a, ssm_lambda_re, ssm_lambda_im, ssm_b_re, ssm_b_im, ssm_c_re, ssm_c_im, ssm_d, ssm_log_dt, w_glu_b, b_glu_b, w_out_ab, pool_w, pool_scale, xa_wq, xa_wkv, xa_wo, ffn_w_up, ffn_conv, ffn_w_down, loss_target, m_norm_mix_g, m_norm_xa_g, m_norm_ffn_g, m_norm_mem_g, m_norm_final_g, m_w_in_ab, m_conv_qkv_a, m_a_log_a, m_dt_bias_a, m_onorm_g_a, m_ssm_lambda_re, m_ssm_lambda_im, m_ssm_b_re, m_ssm_b_im, m_ssm_c_re, m_ssm_c_im, m_ssm_d, m_ssm_log_dt, m_w_glu_b, m_b_glu_b, m_w_out_ab, m_pool_w, m_pool_scale, m_xa_wq, m_xa_wkv, m_xa_wo, m_ffn_w_up, m_ffn_conv, m_ffn_w_down, v_norm_mix_g, v_norm_xa_g, v_norm_ffn_g, v_norm_mem_g, v_norm_final_g, v_w_in_ab, v_conv_qkv_a, v_a_log_a, v_dt_bias_a, v_onorm_g_a, v_ssm_lambda_re, v_ssm_lambda_im, v_ssm_b_re, v_ssm_b_im, v_ssm_c_re, v_ssm_c_im, v_ssm_d, v_ssm_log_dt, v_w_glu_b, v_b_glu_b, v_w_out_ab, v_pool_w, v_pool_scale, v_xa_wq, v_xa_wkv, v_xa_wo, v_ffn_w_up, v_ffn_conv, v_ffn_w_down)
    return _step(dict(zip(INPUT_NAMES, args)))
```

```python
import functools
import math

import numpy as np
import jax
import jax.numpy as jnp
from jax import lax
from jax.experimental import pallas as pl
from jax.experimental.pallas import tpu as pltpu
from jax.experimental.pallas import tpu_sc as plsc

F32, BF16 = jnp.float32, jnp.bfloat16
HIGH, HIGHEST = lax.Precision.HIGH, lax.Precision.HIGHEST
MESH = pl.DeviceIdType.MESH

N_DEV = 8
SEQ, D_MODEL, MEM_LEN = 2048, 1024, 256
WIDTH_A, N_HEADS_A, HEAD_A, CONV_A = 512, 4, 128, 4
GDR_CHUNK = 128
GDR_HEADS_PER_STEP = 4
SSM_WIDTH, SSM_GROUP, N_GROUPS, SSM_STATE = 512, 16, 32, 64
SSM_CH = N_GROUPS * SSM_STATE
SCAN_CB = 512
POOL_WINDOWS = (2, 4, 8, 16)
POOL_GROUP = 256
N_HEADS_X, HEAD_X = 4, 256
D_FF, CONV_FFN = 2816, 3
RMS_EPS = 1e-6
ADAM_LR, ADAM_B1, ADAM_B2, ADAM_EPS, ADAM_WD, ADAM_STEP = 0.001, 0.9, 0.999, 1e-08, 0.01, 10
LANE = 128
PACK_COLS = 1024
VMEM_LIMIT_BYTES = 56 * 1024 * 1024


def _params(sem=None):
    return pltpu.CompilerParams(dimension_semantics=sem, vmem_limit_bytes=VMEM_LIMIT_BYTES)


class Comm:
    def __init__(self, inputs, out_shapes, sems, start, end, mid=None):
        self.inputs, self.out_shapes, self.sems = list(inputs), list(out_shapes), list(sems)
        self.start, self.mid, self.end = start, mid, end


def _merge_comms(comms):
    comms = [c for c in comms if c is not None]
    if not comms:
        return None, []
    bounds, ni, no, ns = [], 0, 0, 0
    for c in comms:
        bounds.append((ni, no, ns))
        ni, no, ns = ni + len(c.inputs), no + len(c.out_shapes), ns + len(c.sems)

    def phase(which):
        def run(ins, outs, sems):
            for c, (i0, o0, s0) in zip(comms, bounds):
                fn = getattr(c, which)
                if fn is not None:
                    fn(ins[i0:i0 + len(c.inputs)], outs[o0:o0 + len(c.out_shapes)], sems[s0:s0 + len(c.sems)])
        return run

    merged = Comm([a for c in comms for a in c.inputs], [s for c in comms for s in c.out_shapes],
                  [s for c in comms for s in c.sems], phase("start"), phase("end"), phase("mid"))
    return merged, [(o0, o0 + len(c.out_shapes)) for c, (_, o0, _) in zip(comms, bounds)]


def _call(body, *, name, grid, in_specs, out_specs, out_shape, args, scratch_shapes=(), sem=None, comm=None):
    single = not isinstance(out_shape, (list, tuple))
    out_specs_l = [out_specs] if single else list(out_specs)
    out_shape_l = [out_shape] if single else list(out_shape)
    scratch_shapes = list(scratch_shapes)
    merged, spans = _merge_comms(comm if isinstance(comm, (list, tuple)) else [comm])
    if merged is None:
        outs = pl.pallas_call(body, name=name, grid=grid, in_specs=list(in_specs), out_specs=out_specs_l,
                              out_shape=out_shape_l, scratch_shapes=scratch_shapes, compiler_params=_params(sem))(*args)
        outs = outs[0] if single else outs
        return outs if comm is None else (outs, [])
    n_in, n_out, n_scr = len(in_specs), len(out_specs_l), len(scratch_shapes)
    ci, co = len(merged.inputs), len(merged.out_shapes)
    total = int(np.prod(grid))

    def wrapped(*refs):
        ins, cins = refs[:n_in], refs[n_in:n_in + ci]
        outs, couts = refs[n_in + ci:n_in + ci + n_out], refs[n_in + ci + n_out:n_in + ci + n_out + co]
        scr, csems = refs[n_in + ci + n_out + co:n_in + ci + n_out + co + n_scr], refs[n_in + ci + n_out + co + n_scr:]
        lin = pl.program_id(0)
        for d in range(1, len(grid)):
            lin = lin * grid[d] + pl.program_id(d)
        pl.when(lin == 0)(lambda: merged.start(cins, couts, csems))
        body(*ins, *outs, *scr)
        mid_step = min((3 * total) // 4, total - 1)
        pl.when(lin == mid_step)(lambda: merged.mid(cins, couts, csems))
        pl.when(lin == total - 1)(lambda: merged.end(cins, couts, csems))

    any_spec = pl.BlockSpec(memory_space=pl.ANY)
    res = pl.pallas_call(
        wrapped, name=name, grid=grid, in_specs=list(in_specs) + [any_spec] * ci,
        out_specs=out_specs_l + [any_spec] * co, out_shape=out_shape_l + merged.out_shapes,
        scratch_shapes=scratch_shapes + merged.sems,
        compiler_params=_params(("arbitrary",) * len(grid)))(*args, *merged.inputs)
    outs, couts = res[:n_out], res[n_out:]
    return (outs[0] if single else list(outs)), [list(couts[a:b]) for a, b in spans]


def _comm_only(comm, name):
    def body():
        pass

    _, couts = _call(body, name=name, grid=(1,), in_specs=[], out_specs=[], out_shape=[], args=[], comm=comm)
    return couts[0]


def _tile(dim, pref):
    best = None
    for t in range(LANE, min(dim, pref) + 1, LANE):
        if dim % t == 0:
            best = t
    return best if best is not None else dim


MM_VMEM_BUDGET = 40 * 1024 * 1024


def _mm_tiles(m, n, k, a_bytes, b_bytes, o_bytes, r_bytes):
    for tk in (k, _tile(k, 2048), _tile(k, 1024), _tile(k, 512)):
        for tm, tn in ((1024, 1536), (1024, 1024), (1024, 512), (512, 512), (256, 512), (256, 256)):
            tm, tn = _tile(m, tm), _tile(n, tn)
            acc = 0 if tk == k else tm * tn * 4
            need = 2 * (tm * tk * a_bytes + tk * tn * b_bytes + tm * tn * (o_bytes + r_bytes)) + acc
            if need <= MM_VMEM_BUDGET:
                return tm, tn, tk
    raise ValueError("no matmul tiling fits VMEM")


def _mm(a, b, mode, name, out_dtype=F32, res=None, comm=None):
    if mode == "nn":
        (m, k), n = a.shape, b.shape[1]
    elif mode == "nt":
        (m, k), n = a.shape, b.shape[0]
    else:
        (k, m), n = a.shape, b.shape[1]
    tm, tn, tk = _mm_tiles(m, n, k, a.dtype.itemsize, b.dtype.itemsize, jnp.dtype(out_dtype).itemsize,
                           0 if res is None else res.dtype.itemsize)
    nk = k // tk
    dims = {"nn": ((1,), (0,)), "nt": ((1,), (1,)), "tn": ((0,), (0,))}[mode]

    def body(*refs):
        if res is None:
            a_ref, b_ref, o_ref = refs[:3]
            r_ref = None
        else:
            a_ref, b_ref, r_ref, o_ref = refs[:4]
        part = lax.dot_general(a_ref[...].astype(BF16), b_ref[...].astype(BF16), (dims, ((), ())),
                               preferred_element_type=F32)

        def finish(out):
            if r_ref is not None:
                out = out + r_ref[...].astype(F32)
            o_ref[...] = out.astype(out_dtype)

        if nk == 1:
            finish(part)
            return
        acc = refs[-1]
        kk = pl.program_id(2)

        @pl.when(kk == 0)
        def _():
            acc[...] = part

        @pl.when(kk > 0)
        def _():
            acc[...] += part

        @pl.when(kk == nk - 1)
        def _():
            finish(acc[...])

    a_spec = (pl.BlockSpec((tk, tm), lambda i, j, q: (q, i)) if mode == "tn"
              else pl.BlockSpec((tm, tk), lambda i, j, q: (i, q)))
    b_spec = (pl.BlockSpec((tn, tk), lambda i, j, q: (j, q)) if mode == "nt"
              else pl.BlockSpec((tk, tn), lambda i, j, q: (q, j)))
    o_spec = pl.BlockSpec((tm, tn), lambda i, j, q: (i, j))
    in_specs, args = [a_spec, b_spec], [a, b]
    if res is not None:
        in_specs.append(o_spec)
        args.append(res)
    return _call(body, name=name, grid=(m // tm, n // tn, nk), in_specs=in_specs, out_specs=o_spec,
                 out_shape=jax.ShapeDtypeStruct((m, n), out_dtype),
                 scratch_shapes=[] if nk == 1 else [pltpu.VMEM((tm, tn), F32)],
                 sem=("parallel", "parallel", "arbitrary"), args=args, comm=comm)


def _mm_bd(a, b, mode, name, out_dtype=F32, res=None, comm=None, tm=1024):
    if mode == "tn":
        k = a.shape[0]
        nb = min(a.shape[1], b.shape[1]) // LANE
        ma, n = a.shape[1] // nb, b.shape[1] // nb

        def body(a_ref, b_ref, o_ref):
            o_ref[0] = lax.dot_general(a_ref[...].astype(BF16), b_ref[...].astype(BF16), (((0,), (0,)), ((), ())),
                                       preferred_element_type=F32).astype(out_dtype)

        return _call(body, name=name, grid=(nb,),
                     in_specs=[pl.BlockSpec((k, ma), lambda j: (0, j)), pl.BlockSpec((k, n), lambda j: (0, j))],
                     out_specs=pl.BlockSpec((1, ma, n), lambda j: (j, 0, 0)),
                     out_shape=jax.ShapeDtypeStruct((nb, ma, n), out_dtype), sem=("parallel",), args=(a, b), comm=comm)
    m = a.shape[0]
    nb = b.shape[0]
    ka = a.shape[1] // nb
    n = b.shape[2] if mode == "nn" else b.shape[1]
    tm = _tile(m, tm)
    dims = ((1,), (0,)) if mode == "nn" else ((1,), (1,))

    def body(*refs):
        if res is None:
            a_ref, b_ref, o_ref = refs
            r_ref = None
        else:
            a_ref, b_ref, r_ref, o_ref = refs
        out = lax.dot_general(a_ref[...].astype(BF16), b_ref[0].astype(BF16), (dims, ((), ())),
                              preferred_element_type=F32)
        if r_ref is not None:
            out = out + r_ref[...].astype(F32)
        o_ref[...] = out.astype(out_dtype)

    o_spec = pl.BlockSpec((tm, n), lambda i, j: (i, j))
    in_specs = [pl.BlockSpec((tm, ka), lambda i, j: (i, j)), pl.BlockSpec((1,) + b.shape[1:], lambda i, j: (j, 0, 0))]
    args = [a, b]
    if res is not None:
        in_specs.append(o_spec)
        args.append(res)
    return _call(body, name=name, grid=(m // tm, nb), in_specs=in_specs, out_specs=o_spec,
                 out_shape=jax.ShapeDtypeStruct((m, nb * n), out_dtype), sem=("parallel", "parallel"),
                 args=args, comm=comm)


def _rms_fwd(x, g, out_dtype, name, tr=256):
    rows, d = x.shape

    def body(x_ref, g_ref, o_ref):
        xv = x_ref[...]
        r = lax.rsqrt(jnp.mean(xv * xv, axis=-1, keepdims=True) + RMS_EPS)
        o_ref[...] = (xv * r * g_ref[...]).astype(out_dtype)

    return pl.pallas_call(
        body, name=name, grid=(rows // tr,),
        in_specs=[pl.BlockSpec((tr, d), lambda i: (i, 0)), pl.BlockSpec((1, d), lambda i: (0, 0))],
        out_specs=pl.BlockSpec((tr, d), lambda i: (i, 0)), out_shape=jax.ShapeDtypeStruct((rows, d), out_dtype),
        compiler_params=_params(("parallel",)))(x, g.reshape(1, d))


def _rms_bwd(x, g, dy, dres, name, tr=256, comm=None):
    rows, d = x.shape

    def body(*refs):
        if dres is None:
            x_ref, g_ref, dy_ref, dx_ref, dg_ref = refs
            r_ref = None
        else:
            x_ref, g_ref, dy_ref, r_ref, dx_ref, dg_ref = refs

        @pl.when(pl.program_id(0) == 0)
        def _():
            dg_ref[...] = jnp.zeros_like(dg_ref)

        xv, dyv = x_ref[...], dy_ref[...].astype(F32)
        r = lax.rsqrt(jnp.mean(xv * xv, axis=-1, keepdims=True) + RMS_EPS)
        xh = xv * r
        dyg = dyv * g_ref[...]
        dx = r * (dyg - xh * jnp.mean(dyg * xh, axis=-1, keepdims=True))
        if r_ref is not None:
            dx = dx + r_ref[...]
        dx_ref[...] = dx
        dg_ref[...] += jnp.sum(dyv * xh, axis=0, keepdims=True)

    blk = pl.BlockSpec((tr, d), lambda i: (i, 0))
    vec = pl.BlockSpec((1, d), lambda i: (0, 0))
    in_specs, args = [blk, vec, blk], [x, g.reshape(1, d), dy]
    if dres is not None:
        in_specs.append(blk)
        args.append(dres)
    return _call(
        body, name=name, grid=(rows // tr,), in_specs=in_specs, out_specs=[blk, vec],
        out_shape=[jax.ShapeDtypeStruct((rows, d), F32), jax.ShapeDtypeStruct((1, d), F32)],
        sem=("arbitrary",), args=args, comm=comm)


def _loss_head(x, g, target, name, tr=256):
    rows, d = x.shape

    def body(x_ref, g_ref, t_ref, loss_ref, dx_ref, dg_ref):
        @pl.when(pl.program_id(0) == 0)
        def _():
            dg_ref[...] = jnp.zeros_like(dg_ref)
            loss_ref[...] = jnp.zeros_like(loss_ref)

        xv = x_ref[...]
        r = lax.rsqrt(jnp.mean(xv * xv, axis=-1, keepdims=True) + RMS_EPS)
        xh = xv * r
        err = xh * g_ref[...] - t_ref[...]
        loss_ref[...] += 0.5 * jnp.sum(jnp.mean(err * err, axis=-1, keepdims=True), keepdims=True)
        dyv = err * (1.0 / d)
        dyg = dyv * g_ref[...]
        dx_ref[...] = r * (dyg - xh * jnp.mean(dyg * xh, axis=-1, keepdims=True))
        dg_ref[...] += jnp.sum(dyv * xh, axis=0, keepdims=True)

    blk = pl.BlockSpec((tr, d), lambda i: (i, 0))
    vec = pl.BlockSpec((1, d), lambda i: (0, 0))
    return pl.pallas_call(
        body, name=name, grid=(rows // tr,), in_specs=[blk, vec, blk],
        out_specs=[pl.BlockSpec((1, 1), lambda i: (0, 0)), blk, vec],
        out_shape=[jax.ShapeDtypeStruct((1, 1), F32), jax.ShapeDtypeStruct((rows, d), F32),
                   jax.ShapeDtypeStruct((1, d), F32)],
        compiler_params=_params(("arbitrary",)))(x, g.reshape(1, d), target)


def _shift_down(x, s):
    rows = lax.broadcasted_iota(jnp.int32, x.shape, 0)
    return jnp.where(rows >= s, pltpu.roll(x, s, 0), 0.0)


def _shift_up(x, s):
    n = x.shape[0]
    rows = lax.broadcasted_iota(jnp.int32, x.shape, 0)
    return jnp.where(rows < n - s, pltpu.roll(x, n - s, 0), 0.0)


def _sigmoid(x):
    return 1.0 / (1.0 + jnp.exp(-x))


def _silu_and_grad(x):
    s = _sigmoid(x)
    return x * s, s * (1.0 + x * (1.0 - s))


_GELU_C0, _GELU_C1 = math.sqrt(2.0 / math.pi), 0.044715


def _gelu_and_grad(x):
    th = jnp.tanh(_GELU_C0 * (x + _GELU_C1 * x * x * x))
    y = 0.5 * x * (1.0 + th)
    dy = 0.5 * (1.0 + th) + 0.5 * x * (1.0 - th * th) * _GELU_C0 * (1.0 + 3.0 * _GELU_C1 * x * x)
    return y, dy


def _ffn_act_fwd(h, w, name, tc=256, comm=None):
    t = h.shape[0]
    nb = D_FF // tc

    def body(hg_ref, hv_ref, wg_ref, wv_ref, a_ref):
        def conv(x, wr):
            return wr[2:3, :] * x + wr[1:2, :] * _shift_down(x, 1) + wr[0:1, :] * _shift_down(x, 2)

        cg = conv(hg_ref[...], wg_ref[...])
        cv = conv(hv_ref[...], wv_ref[...])
        a_ref[...] = (cg * _sigmoid(cg) * cv).astype(BF16)

    return _call(
        body, name=name, grid=(nb,),
        in_specs=[pl.BlockSpec((t, tc), lambda j: (0, j)), pl.BlockSpec((t, tc), lambda j: (0, j + nb)),
                  pl.BlockSpec((CONV_FFN, tc), lambda j: (0, j)), pl.BlockSpec((CONV_FFN, tc), lambda j: (0, j + nb))],
        out_specs=pl.BlockSpec((t, tc), lambda j: (0, j)), out_shape=jax.ShapeDtypeStruct((t, D_FF), BF16),
        sem=("parallel",), args=(h, h, w, w), comm=comm)


def _ffn_act_bwd(h, w, da, name, tc=256, comm=None):
    t = h.shape[0]
    nb = D_FF // tc

    def body(hg_ref, hv_ref, wg_ref, wv_ref, da_ref, dhg_ref, dhv_ref, dwg_ref, dwv_ref):
        hg, hv, wg, wv = hg_ref[...], hv_ref[...], wg_ref[...], wv_ref[...]
        hg1, hg2, hv1, hv2 = _shift_down(hg, 1), _shift_down(hg, 2), _shift_down(hv, 1), _shift_down(hv, 2)
        cg = wg[2:3, :] * hg + wg[1:2, :] * hg1 + wg[0:1, :] * hg2
        cv = wv[2:3, :] * hv + wv[1:2, :] * hv1 + wv[0:1, :] * hv2
        sg, dsg = _silu_and_grad(cg)
        dav = da_ref[...].astype(F32)
        dcv = dav * sg
        dcg = dav * cv * dsg

        def conv_t(dc, wr):
            return wr[2:3, :] * dc + wr[1:2, :] * _shift_up(dc, 1) + wr[0:1, :] * _shift_up(dc, 2)

        dhg_ref[...] = conv_t(dcg, wg).astype(BF16)
        dhv_ref[...] = conv_t(dcv, wv).astype(BF16)
        dwg_ref[0:1, :] = jnp.sum(dcg * hg2, axis=0, keepdims=True)
        dwg_ref[1:2, :] = jnp.sum(dcg * hg1, axis=0, keepdims=True)
        dwg_ref[2:3, :] = jnp.sum(dcg * hg, axis=0, keepdims=True)
        dwv_ref[0:1, :] = jnp.sum(dcv * hv2, axis=0, keepdims=True)
        dwv_ref[1:2, :] = jnp.sum(dcv * hv1, axis=0, keepdims=True)
        dwv_ref[2:3, :] = jnp.sum(dcv * hv, axis=0, keepdims=True)

    big = lambda off: pl.BlockSpec((t, tc), lambda j: (0, j + off))
    small = lambda off: pl.BlockSpec((CONV_FFN, tc), lambda j: (0, j + off))
    res = _call(
        body, name=name, grid=(nb,),
        in_specs=[big(0), big(nb), small(0), small(nb), big(0)],
        out_specs=[big(0), big(0), small(0), small(0)],
        out_shape=[jax.ShapeDtypeStruct((t, D_FF), BF16), jax.ShapeDtypeStruct((t, D_FF), BF16),
                   jax.ShapeDtypeStruct((CONV_FFN, D_FF), F32), jax.ShapeDtypeStruct((CONV_FFN, D_FF), F32)],
        sem=("parallel",), args=(h, h, w, w, da), comm=comm)
    (dhg, dhv, dwg, dwv), couts = res if comm is not None else (res, None)
    out = (jnp.concatenate([dhg, dhv], axis=1), jnp.concatenate([dwg, dwv], axis=1))
    return out if comm is None else (out, couts)


def _attn_probs(q, k):
    s = lax.dot_general(q.astype(BF16), k.astype(BF16), (((1,), (1,)), ((), ())),
                        preferred_element_type=F32) * (HEAD_X ** -0.5)
    s = s - jnp.max(s, axis=-1, keepdims=True)
    p = jnp.exp(s)
    return p / jnp.sum(p, axis=-1, keepdims=True)


def _attn_fwd(q, kv, name, tq=512, comm=None):
    t = q.shape[0]

    def body(q_ref, k_ref, v_ref, o_ref):
        p = _attn_probs(q_ref[...], k_ref[...])
        o_ref[...] = jnp.dot(p.astype(BF16), v_ref[...].astype(BF16), preferred_element_type=F32).astype(BF16)

    return _call(
        body, name=name, grid=(N_HEADS_X, t // tq),
        in_specs=[pl.BlockSpec((tq, HEAD_X), lambda h, i: (i, h)),
                  pl.BlockSpec((MEM_LEN, HEAD_X), lambda h, i: (0, h)),
                  pl.BlockSpec((MEM_LEN, HEAD_X), lambda h, i: (0, h + N_HEADS_X))],
        out_specs=pl.BlockSpec((tq, HEAD_X), lambda h, i: (i, h)),
        out_shape=jax.ShapeDtypeStruct((t, N_HEADS_X * HEAD_X), BF16),
        sem=("parallel", "parallel"), args=(q, kv, kv), comm=comm)


def _attn_bwd(q, kv, do, name, tq=512):
    t = q.shape[0]

    def body(q_ref, k_ref, v_ref, do_ref, dq_ref, dk_ref, dv_ref):
        @pl.when(pl.program_id(1) == 0)
        def _():
            dk_ref[...] = jnp.zeros_like(dk_ref)
            dv_ref[...] = jnp.zeros_like(dv_ref)

        qb, kb, vb, dob = (r[...].astype(BF16) for r in (q_ref, k_ref, v_ref, do_ref))
        p = _attn_probs(qb, kb)
        dp = lax.dot_general(dob, vb, (((1,), (1,)), ((), ())), preferred_element_type=F32)
        ds = p * (dp - jnp.sum(dp * p, axis=-1, keepdims=True)) * (HEAD_X ** -0.5)
        dsb = ds.astype(BF16)
        dq_ref[...] = jnp.dot(dsb, kb, preferred_element_type=F32).astype(BF16)
        dk_ref[...] += lax.dot_general(dsb, qb, (((0,), (0,)), ((), ())), preferred_element_type=F32)
        dv_ref[...] += lax.dot_general(p.astype(BF16), dob, (((0,), (0,)), ((), ())), preferred_element_type=F32)

    qs = pl.BlockSpec((tq, HEAD_X), lambda h, i: (i, h))
    ms = pl.BlockSpec((MEM_LEN, HEAD_X), lambda h, i: (0, h))
    return pl.pallas_call(
        body, name=name, grid=(N_HEADS_X, t // tq),
        in_specs=[qs, ms, pl.BlockSpec((MEM_LEN, HEAD_X), lambda h, i: (0, h + N_HEADS_X)), qs],
        out_specs=[qs, ms, ms],
        out_shape=[jax.ShapeDtypeStruct((t, D_MODEL), BF16), jax.ShapeDtypeStruct((MEM_LEN, D_MODEL), F32),
                   jax.ShapeDtypeStruct((MEM_LEN, D_MODEL), F32)],
        compiler_params=_params(("parallel", "arbitrary")))(q, kv, kv, do)


def _pool_counts(t, win):
    pos = lax.broadcasted_iota(jnp.int32, (t, 1), 0).astype(F32) + 1.0
    return 1.0 / jnp.minimum(pos, float(win))


def _pool_delta(xv, win):
    s, step = xv, 1
    while step < win:
        s = s + _shift_down(s, step)
        step *= 2
    return s * _pool_counts(xv.shape[0], win) - xv


def _pool_delta_t(dv, win):
    s, step = dv * _pool_counts(dv.shape[0], win), 1
    while step < win:
        s = s + _shift_up(s, step)
        step *= 2
    return s - dv


def _pool_fwd(xn, w, scale, res, name):
    t = xn.shape[0]

    def make_branch(win, xn_ref, w_ref, s_ref, r_ref, o_ref):
        def branch():
            dl = _pool_delta(xn_ref[...], win)
            y = jnp.dot(dl.astype(BF16), w_ref[0], preferred_element_type=F32)
            o_ref[...] = r_ref[...] + y * s_ref[...]
        return branch

    def body(xn_ref, w_ref, s_ref, r_ref, o_ref):
        for gi, win in enumerate(POOL_WINDOWS):
            pl.when(pl.program_id(0) == gi)(make_branch(win, xn_ref, w_ref, s_ref, r_ref, o_ref))

    blk = pl.BlockSpec((t, POOL_GROUP), lambda g: (0, g))
    return pl.pallas_call(
        body, name=name, grid=(len(POOL_WINDOWS),),
        in_specs=[blk, pl.BlockSpec((1, POOL_GROUP, POOL_GROUP), lambda g: (g, 0, 0)),
                  pl.BlockSpec((1, POOL_GROUP), lambda g: (0, g)), blk],
        out_specs=blk, out_shape=jax.ShapeDtypeStruct((t, D_MODEL), F32),
        compiler_params=_params(("parallel",)))(xn, w, scale, res)


def _pool_bwd(xn, w, scale, dmix, name):
    t = xn.shape[0]

    def make_branch(win, xn_ref, w_ref, s_ref, d_ref, dxn_ref, dw_ref, ds_ref):
        def branch():
            dl = _pool_delta(xn_ref[...], win).astype(BF16)
            wv = w_ref[0]
            dm = d_ref[...]
            y = jnp.dot(dl, wv, preferred_element_type=F32)
            ds_ref[...] = jnp.sum(dm * y, axis=0, keepdims=True)
            dy = (dm * s_ref[...]).astype(BF16)
            dw_ref[0] = lax.dot_general(dl, dy, (((0,), (0,)), ((), ())), preferred_element_type=F32)
            ddl = lax.dot_general(dy, wv, (((1,), (1,)), ((), ())), preferred_element_type=F32)
            dxn_ref[...] = _pool_delta_t(ddl, win)
        return branch

    def body(*refs):
        for gi, win in enumerate(POOL_WINDOWS):
            pl.when(pl.program_id(0) == gi)(make_branch(win, *refs))

    blk = pl.BlockSpec((t, POOL_GROUP), lambda g: (0, g))
    wspec = pl.BlockSpec((1, POOL_GROUP, POOL_GROUP), lambda g: (g, 0, 0))
    vec = pl.BlockSpec((1, POOL_GROUP), lambda g: (0, g))
    return pl.pallas_call(
        body, name=name, grid=(len(POOL_WINDOWS),), in_specs=[blk, wspec, vec, blk], out_specs=[blk, wspec, vec],
        out_shape=[jax.ShapeDtypeStruct((t, D_MODEL), F32),
                   jax.ShapeDtypeStruct((len(POOL_WINDOWS), POOL_GROUP, POOL_GROUP), F32),
                   jax.ShapeDtypeStruct((1, D_MODEL), F32)],
        compiler_params=_params(("parallel",)))(xn, w, scale, dmix)


def _qkv_conv(h, wr):
    return (wr[3:4, :] * h + wr[2:3, :] * _shift_down(h, 1) + wr[1:2, :] * _shift_down(h, 2)
            + wr[0:1, :] * _shift_down(h, 3))


def _qkv_pre_fwd(h, w, col0, ncols, normalize, scale, name):
    t = h.shape[0]

    def body(h_ref, w_ref, o_ref):
        c = _qkv_conv(h_ref[...], w_ref[...])
        s = c * _sigmoid(c)
        if normalize:
            s = s * lax.rsqrt(jnp.sum(s * s, axis=-1, keepdims=True) + 1e-6) * scale
        o_ref[...] = s

    return pl.pallas_call(
        body, name=name, grid=(ncols,),
        in_specs=[pl.BlockSpec((t, HEAD_A), lambda j: (0, j + col0)), pl.BlockSpec((CONV_A, HEAD_A), lambda j: (0, j + col0))],
        out_specs=pl.BlockSpec((t, HEAD_A), lambda j: (0, j)), out_shape=jax.ShapeDtypeStruct((t, ncols * HEAD_A), F32),
        compiler_params=_params(("parallel",)))(h, w)


def _qkv_pre_bwd(h, w, dy, col0, ncols, normalize, scale, name):
    t = h.shape[0]

    def body(h_ref, w_ref, dy_ref, dh_ref, dw_ref):
        hv, wr, dyv = h_ref[...], w_ref[...], dy_ref[...]
        h1, h2, h3 = _shift_down(hv, 1), _shift_down(hv, 2), _shift_down(hv, 3)
        c = wr[3:4, :] * hv + wr[2:3, :] * h1 + wr[1:2, :] * h2 + wr[0:1, :] * h3
        s, dsilu = _silu_and_grad(c)
        if normalize:
            r = lax.rsqrt(jnp.sum(s * s, axis=-1, keepdims=True) + 1e-6)
            y = s * r
            dyv = dyv * scale
            ds = r * (dyv - y * jnp.sum(dyv * y, axis=-1, keepdims=True))
        else:
            ds = dyv
        dc = ds * dsilu
        dh = (wr[3:4, :] * dc + wr[2:3, :] * _shift_up(dc, 1) + wr[1:2, :] * _shift_up(dc, 2)
              + wr[0:1, :] * _shift_up(dc, 3))
        dh_ref[...] = dh.astype(BF16)
        dw_ref[0:1, :] = jnp.sum(dc * h3, axis=0, keepdims=True)
        dw_ref[1:2, :] = jnp.sum(dc * h2, axis=0, keepdims=True)
        dw_ref[2:3, :] = jnp.sum(dc * h1, axis=0, keepdims=True)
        dw_ref[3:4, :] = jnp.sum(dc * hv, axis=0, keepdims=True)

    return pl.pallas_call(
        body, name=name, grid=(ncols,),
        in_specs=[pl.BlockSpec((t, HEAD_A), lambda j: (0, j + col0)), pl.BlockSpec((CONV_A, HEAD_A), lambda j: (0, j + col0)),
                  pl.BlockSpec((t, HEAD_A), lambda j: (0, j))],
        out_specs=[pl.BlockSpec((t, HEAD_A), lambda j: (0, j)), pl.BlockSpec((CONV_A, HEAD_A), lambda j: (0, j))],
        out_shape=[jax.ShapeDtypeStruct((t, ncols * HEAD_A), BF16), jax.ShapeDtypeStruct((CONV_A, ncols * HEAD_A), F32)],
        compiler_params=_params(("parallel",)))(h, w, dy)


def _softplus(x):
    return jnp.maximum(x, 0.0) + jnp.log1p(jnp.exp(-jnp.abs(x)))


def _gates_fwd(ba, arow, brow, name):
    t = ba.shape[0]

    def body(x_ref, a_ref, b_ref, o_ref):
        xv = x_ref[...]
        lane = lax.broadcasted_iota(jnp.int32, xv.shape, 1)
        beta = _sigmoid(xv)
        g = -jnp.exp(a_ref[...]) * _softplus(xv + b_ref[...])
        o_ref[...] = jnp.where(lane < N_HEADS_A, beta, jnp.where(lane < 2 * N_HEADS_A, g, 0.0))

    return pl.pallas_call(body, name=name, out_shape=jax.ShapeDtypeStruct((t, LANE), F32),
                          compiler_params=_params())(ba, arow, brow)


def _gates_bwd(ba, arow, brow, dgb, name):
    t = ba.shape[0]

    def body(x_ref, a_ref, b_ref, d_ref, dx_ref, da_ref, db_ref):
        xv = x_ref[...]
        dv = d_ref[0] + d_ref[1] + d_ref[2] + d_ref[3]
        lane = lax.broadcasted_iota(jnp.int32, xv.shape, 1)
        beta = _sigmoid(xv)
        ea = jnp.exp(a_ref[...])
        z = xv + b_ref[...]
        dgv = jnp.where((lane >= N_HEADS_A) & (lane < 2 * N_HEADS_A), dv, 0.0) * (-ea)
        dz = dgv * _sigmoid(z)
        dx = jnp.where(lane < N_HEADS_A, dv * beta * (1.0 - beta), dz)
        dx_ref[...] = dx.astype(BF16)
        db_ref[...] = jnp.sum(dz, axis=0, keepdims=True)
        da_ref[...] = jnp.sum(dgv * _softplus(z), axis=0, keepdims=True)

    return pl.pallas_call(
        body, name=name,
        out_shape=[jax.ShapeDtypeStruct((t, LANE), BF16), jax.ShapeDtypeStruct((1, LANE), F32),
                   jax.ShapeDtypeStruct((1, LANE), F32)],
        compiler_params=_params())(ba, arow, brow, dgb)


def _dot(a, b, prec=None):
    if prec is None:
        return jnp.dot(a.astype(BF16), b.astype(BF16), preferred_element_type=F32)
    return jnp.dot(a, b, precision=prec, preferred_element_type=F32)


def _dot_nt(a, b, prec=None):
    if prec is None:
        a, b = a.astype(BF16), b.astype(BF16)
    return lax.dot_general(a, b, (((1,), (1,)), ((), ())), precision=prec, preferred_element_type=F32)


def _dot_tn(a, b, prec=None):
    if prec is None:
        a, b = a.astype(BF16), b.astype(BF16)
    return lax.dot_general(a, b, (((0,), (0,)), ((), ())), precision=prec, preferred_element_type=F32)


def _gdr_chunk_terms(k, beta, g):
    c = GDR_CHUNK
    row = lax.broadcasted_iota(jnp.int32, (c, c), 0)
    col = lax.broadcasted_iota(jnp.int32, (c, c), 1)
    causal, strict = row >= col, row > col
    gcum = _dot(causal.astype(F32), jnp.broadcast_to(g, (c, c)), HIGHEST)
    diff = gcum - gcum.T
    decay = jnp.where(causal, jnp.exp(jnp.where(causal, diff, 0.0)), 0.0)
    kb = k * beta
    kk = _dot_nt(kb, k)
    return row, col, causal, strict, gcum, decay, kb, kk


def _unit_lower_inverse(a):
    c = a.shape[0]
    eye = (lax.broadcasted_iota(jnp.int32, (c, c), 0) == lax.broadcasted_iota(jnp.int32, (c, c), 1)).astype(F32)
    p = -a
    inv = eye + p
    step = 1
    while 2 * step < c:
        p = _dot(p, p, HIGH)
        inv = inv + _dot(inv, p, HIGH)
        step *= 2
    return inv


def _head_gates(gates, head):
    lane = lax.broadcasted_iota(jnp.int32, gates.shape, 1)
    beta = jnp.sum(jnp.where(lane == head, gates, 0.0), axis=1, keepdims=True)
    g = jnp.sum(jnp.where(lane == head + N_HEADS_A, gates, 0.0), axis=1, keepdims=True)
    return beta, g


def _gdr_fwd(q, k, v, gates, name, comm=None):
    t = q.shape[0]
    c = GDR_CHUNK
    n = t // c

    hps = GDR_HEADS_PER_STEP

    def one_head(hh, q_ref, k_ref, v_ref, gb_ref, o_ref, tm_ref, s_ref, state):
        cols = slice(hh * HEAD_A, (hh + 1) * HEAD_A)
        qv, kv, vv = q_ref[:, cols], k_ref[:, cols], v_ref[:, cols]
        beta, g = _head_gates(gb_ref[...], pl.program_id(0) * hps + hh)
        row, col, causal, strict, gcum, decay, kb, kk = _gdr_chunk_terms(kv, beta, g)
        tm = _unit_lower_inverse(jnp.where(strict, kk * decay, 0.0))
        e = jnp.exp(gcum)
        u = _dot(tm, vv * beta, HIGH)
        w = _dot(tm, kb * e, HIGH)
        p = jnp.where(causal, _dot_nt(qv, kv) * decay, 0.0)
        s = state[hh]
        s_ref[hh, 0] = s
        tm_ref[hh, 0] = tm
        vn = u - _dot(w, s)
        o_ref[:, cols] = _dot(qv * e, s) + _dot(p, vn)
        glast = gcum[c - 1:c, :]
        state[hh] = s * jnp.exp(glast) + _dot_tn(kv * jnp.exp(glast - gcum), vn)

    def body(*refs):
        state = refs[-1]

        @pl.when(pl.program_id(1) == 0)
        def _():
            state[...] = jnp.zeros_like(state)

        for hh in range(hps):
            one_head(hh, *refs)

    blk = pl.BlockSpec((c, hps * HEAD_A), lambda h, i: (i, h))
    mat = pl.BlockSpec((hps, 1, c, c), lambda h, i: (h, i, 0, 0))
    return _call(
        body, name=name, grid=(N_HEADS_A // hps, n),
        in_specs=[blk, blk, blk, pl.BlockSpec((c, LANE), lambda h, i: (i, 0))],
        out_specs=[blk, mat, mat],
        out_shape=[jax.ShapeDtypeStruct((t, WIDTH_A), F32), jax.ShapeDtypeStruct((N_HEADS_A, n, c, c), F32),
                   jax.ShapeDtypeStruct((N_HEADS_A, n, HEAD_A, HEAD_A), F32)],
        scratch_shapes=[pltpu.VMEM((hps, HEAD_A, HEAD_A), F32)], sem=("parallel", "arbitrary"),
        args=(q, k, v, gates), comm=comm)


def _gdr_bwd(q, k, v, gates, tm_all, s_all, do, name, comm=None):
    t = q.shape[0]
    c = GDR_CHUNK
    n = t // c

    hps = GDR_HEADS_PER_STEP

    def one_head(hh, q_ref, k_ref, v_ref, gb_ref, tm_ref, s_ref, do_ref, dq_ref, dk_ref, dv_ref, dgb_ref, dstate):
        cols = slice(hh * HEAD_A, (hh + 1) * HEAD_A)
        qv, kv, vv, dov = q_ref[:, cols], k_ref[:, cols], v_ref[:, cols], do_ref[:, cols]
        head = pl.program_id(0) * hps + hh
        beta, g = _head_gates(gb_ref[...], head)
        tm, s, dsp = tm_ref[hh, 0], s_ref[hh, 0], dstate[hh]
        row, col, causal, strict, gcum, decay, kb, kk = _gdr_chunk_terms(kv, beta, g)
        e = jnp.exp(gcum)
        vb, kbe = vv * beta, kb * e
        u = _dot(tm, vb, HIGH)
        w = _dot(tm, kbe, HIGH)
        qk = _dot_nt(qv, kv)
        p = jnp.where(causal, qk * decay, 0.0)
        vn = u - _dot(w, s)
        glast = gcum[c - 1:c, :]
        el = jnp.exp(glast)
        f = jnp.exp(glast - gcum)
        kd = kv * f
        qe = qv * e

        dvn = _dot_tn(p, dov) + _dot(kd, dsp)
        dglast = el[:, 0:1] * jnp.sum(s * dsp, keepdims=True)
        dkd = _dot_nt(vn, dsp)
        dk = dkd * f
        df = jnp.sum(dkd * kv, axis=1, keepdims=True) * f[:, 0:1]
        dglast = dglast + jnp.sum(df, keepdims=True)
        dgc = -df
        dp = jnp.where(causal, _dot_nt(dov, vn), 0.0)
        dqe = _dot_nt(dov, s)
        dq = dqe * e
        de = jnp.sum(dqe * qv, axis=1, keepdims=True)
        dstate[hh] = dsp * el + _dot_tn(qe, dov) - _dot_tn(w, dvn)
        dw = -_dot_nt(dvn, s)
        dvb = _dot_tn(tm, dvn, HIGH)
        dkbe = _dot_tn(tm, dw, HIGH)
        da = -jnp.where(strict, _dot_nt(dvb, u) + _dot_nt(dkbe, w), 0.0)
        dkk = da * decay
        dqk = dp * decay
        dd = da * kk + dp * qk
        dq = dq + _dot(dqk, kv)
        dk = dk + _dot_tn(dqk, qv)
        dkb = _dot(dkk, kv) + dkbe * e
        dk = dk + _dot_tn(dkk, kb)
        de = de + jnp.sum(dkbe * kb, axis=1, keepdims=True)
        dk = dk + dkb * beta
        dbeta = jnp.sum(dkb * kv, axis=1, keepdims=True) + jnp.sum(dvb * vv, axis=1, keepdims=True)
        m = dd * decay
        dgc = dgc + jnp.sum(m, axis=1, keepdims=True) - jnp.sum(m.T, axis=1, keepdims=True)
        dgc = dgc + de * e[:, 0:1]
        dgc = dgc + jnp.where(row[:, 0:1] == c - 1, dglast, 0.0)
        dg = _dot((row <= col).astype(F32), jnp.broadcast_to(dgc, (c, c)), HIGHEST)
        dq_ref[:, cols] = dq
        dk_ref[:, cols] = dk
        dv_ref[:, cols] = dvb * beta
        lane = lax.broadcasted_iota(jnp.int32, (c, LANE), 1)
        dgb_ref[hh] = jnp.where(lane == head, dbeta, jnp.where(lane == head + N_HEADS_A, dg, 0.0))

    def body(*refs):
        dstate = refs[-1]

        @pl.when(pl.program_id(1) == 0)
        def _():
            dstate[...] = jnp.zeros_like(dstate)

        for hh in range(hps):
            one_head(hh, *refs)

    blk = pl.BlockSpec((c, hps * HEAD_A), lambda h, i: (n - 1 - i, h))
    mat = pl.BlockSpec((hps, 1, c, c), lambda h, i: (h, n - 1 - i, 0, 0))
    return _call(
        body, name=name, grid=(N_HEADS_A // hps, n),
        in_specs=[blk, blk, blk, pl.BlockSpec((c, LANE), lambda h, i: (n - 1 - i, 0)), mat, mat, blk],
        out_specs=[blk, blk, blk, pl.BlockSpec((hps, c, LANE), lambda h, i: (h, n - 1 - i, 0))],
        out_shape=[jax.ShapeDtypeStruct((t, WIDTH_A), F32)] * 3 + [jax.ShapeDtypeStruct((N_HEADS_A, t, LANE), F32)],
        scratch_shapes=[pltpu.VMEM((hps, HEAD_A, HEAD_A), F32)], sem=("parallel", "arbitrary"),
        args=(q, k, v, gates, tm_all, s_all, do), comm=comm)


_B_NN, _B_NT, _B_TN = ((2,), (1,)), ((2,), (2,)), ((1,), (1,))


def _bdot(a, b, dims=_B_NN, prec=None):
    if prec is None:
        a, b = a.astype(BF16), b.astype(BF16)
    return lax.dot_general(a, b, (dims, ((0,), (0,))), precision=prec, preferred_element_type=F32)


def _heads_of(ref):
    return jnp.stack([ref[:, h * HEAD_A:(h + 1) * HEAD_A] for h in range(N_HEADS_A)])


def _all_head_gates(gates):
    pairs = [_head_gates(gates, h) for h in range(N_HEADS_A)]
    return jnp.stack([b for b, _ in pairs]), jnp.stack([g for _, g in pairs])


def _gdr_terms(k, beta, g):
    h, c = k.shape[0], GDR_CHUNK
    row = lax.broadcasted_iota(jnp.int32, (c, c), 0)
    col = lax.broadcasted_iota(jnp.int32, (c, c), 1)
    causal, strict = row >= col, row > col
    lower = jnp.broadcast_to(causal.astype(F32), (h, c, c))
    gcum = _bdot(lower, jnp.broadcast_to(g, (h, c, c)), prec=HIGHEST)
    diff = gcum - jnp.swapaxes(gcum, 1, 2)
    decay = jnp.where(causal, jnp.exp(jnp.where(causal, diff, 0.0)), 0.0)
    kb = k * beta
    return row, col, causal, strict, gcum, decay, kb, _bdot(kb, k, _B_NT)


def _unit_lower_inverses(a):
    c = a.shape[1]
    eye = (lax.broadcasted_iota(jnp.int32, (c, c), 0) == lax.broadcasted_iota(jnp.int32, (c, c), 1)).astype(F32)
    p = -a
    inv = eye + p
    step = 1
    while 2 * step < c:
        p = _bdot(p, p, prec=HIGH)
        inv = inv + _bdot(inv, p, prec=HIGH)
        step *= 2
    return inv


def _gdr_fwd(q, k, v, gates, name, comm=None):
    t = q.shape[0]
    c, nh = GDR_CHUNK, N_HEADS_A
    n = t // c

    def body(q_ref, k_ref, v_ref, gb_ref, o_ref, tm_ref, s_ref, state):
        @pl.when(pl.program_id(0) == 0)
        def _():
            state[...] = jnp.zeros_like(state)

        qv, kv, vv = _heads_of(q_ref), _heads_of(k_ref), _heads_of(v_ref)
        beta, g = _all_head_gates(gb_ref[...])
        row, col, causal, strict, gcum, decay, kb, kk = _gdr_terms(kv, beta, g)
        tm = _unit_lower_inverses(jnp.where(strict, kk * decay, 0.0))
        e = jnp.exp(gcum)
        u = _bdot(tm, vv * beta, prec=HIGH)
        w = _bdot(tm, kb * e, prec=HIGH)
        p = jnp.where(causal, _bdot(qv, kv, _B_NT) * decay, 0.0)
        s = state[...]
        s_ref[:, 0] = s
        tm_ref[:, 0] = tm
        vn = u - _bdot(w, s)
        o = _bdot(qv * e, s) + _bdot(p, vn)
        for h in range(nh):
            o_ref[:, h * HEAD_A:(h + 1) * HEAD_A] = o[h]
        glast = gcum[:, c - 1:c, :]
        state[...] = s * jnp.exp(glast) + _bdot(kv * jnp.exp(glast - gcum), vn, _B_TN)

    blk = pl.BlockSpec((c, WIDTH_A), lambda i: (i, 0))
    mat = pl.BlockSpec((nh, 1, c, c), lambda i: (0, i, 0, 0))
    return _call(
        body, name=name, grid=(n,), in_specs=[blk, blk, blk, pl.BlockSpec((c, LANE), lambda i: (i, 0))],
        out_specs=[blk, mat, mat],
        out_shape=[jax.ShapeDtypeStruct((t, WIDTH_A), F32), jax.ShapeDtypeStruct((nh, n, c, c), F32),
                   jax.ShapeDtypeStruct((nh, n, HEAD_A, HEAD_A), F32)],
        scratch_shapes=[pltpu.VMEM((nh, HEAD_A, HEAD_A), F32)], sem=("arbitrary",),
        args=(q, k, v, gates), comm=comm)


def _gdr_bwd(q, k, v, gates, tm_all, s_all, do, name, comm=None):
    t = q.shape[0]
    c, nh = GDR_CHUNK, N_HEADS_A
    n = t // c

    def body(q_ref, k_ref, v_ref, gb_ref, tm_ref, s_ref, do_ref, dq_ref, dk_ref, dv_ref, dgb_ref, dstate):
        @pl.when(pl.program_id(0) == 0)
        def _():
            dstate[...] = jnp.zeros_like(dstate)

        qv, kv, vv, dov = _heads_of(q_ref), _heads_of(k_ref), _heads_of(v_ref), _heads_of(do_ref)
        beta, g = _all_head_gates(gb_ref[...])
        tm, s, dsp = tm_ref[:, 0], s_ref[:, 0], dstate[...]
        row, col, causal, strict, gcum, decay, kb, kk = _gdr_terms(kv, beta, g)
        rowsum = lambda x: jnp.sum(x, axis=2, keepdims=True)
        e = jnp.exp(gcum)
        vb, kbe = vv * beta, kb * e
        u = _bdot(tm, vb, prec=HIGH)
        w = _bdot(tm, kbe, prec=HIGH)
        qk = _bdot(qv, kv, _B_NT)
        p = jnp.where(causal, qk * decay, 0.0)
        vn = u - _bdot(w, s)
        glast = gcum[:, c - 1:c, :]
        el = jnp.exp(glast)
        f = jnp.exp(glast - gcum)
        kd = kv * f
        qe = qv * e

        dvn = _bdot(p, dov, _B_TN) + _bdot(kd, dsp)
        dglast = el[:, :, 0:1] * jnp.sum(s * dsp, axis=(1, 2), keepdims=True)
        dkd = _bdot(vn, dsp, _B_NT)
        dk = dkd * f
        df = rowsum(dkd * kv) * f[:, :, 0:1]
        dglast = dglast + jnp.sum(df, axis=1, keepdims=True)
        dgc = -df
        dp = jnp.where(causal, _bdot(dov, vn, _B_NT), 0.0)
        dqe = _bdot(dov, s, _B_NT)
        dq = dqe * e
        de = rowsum(dqe * qv)
        dstate[...] = dsp * el + _bdot(qe, dov, _B_TN) - _bdot(w, dvn, _B_TN)
        dw = -_bdot(dvn, s, _B_NT)
        dvb = _bdot(tm, dvn, _B_TN, prec=HIGH)
        dkbe = _bdot(tm, dw, _B_TN, prec=HIGH)
        da = -jnp.where(strict, _bdot(dvb, u, _B_NT) + _bdot(dkbe, w, _B_NT), 0.0)
        dkk = da * decay
        dqk = dp * decay
        dd = da * kk + dp * qk
        dq = dq + _bdot(dqk, kv)
        dk = dk + _bdot(dqk, qv, _B_TN)
        dkb = _bdot(dkk, kv) + dkbe * e
        dk = dk + _bdot(dkk, kb, _B_TN)
        de = de + rowsum(dkbe * kb)
        dk = dk + dkb * beta
        dbeta = rowsum(dkb * kv) + rowsum(dvb * vv)
        m = dd * decay
        dgc = dgc + rowsum(m) - rowsum(jnp.swapaxes(m, 1, 2))
        dgc = dgc + de * e[:, :, 0:1]
        dgc = dgc + jnp.where(row[:, 0:1] == c - 1, dglast, 0.0)
        upper = jnp.broadcast_to((row <= col).astype(F32), (nh, c, c))
        dg = _bdot(upper, jnp.broadcast_to(dgc, (nh, c, c)), prec=HIGHEST)
        dv = dvb * beta
        for h in range(nh):
            cols = slice(h * HEAD_A, (h + 1) * HEAD_A)
            dq_ref[:, cols] = dq[h]
            dk_ref[:, cols] = dk[h]
            dv_ref[:, cols] = dv[h]
        head = lax.broadcasted_iota(jnp.int32, (nh, c, LANE), 0)
        lane = lax.broadcasted_iota(jnp.int32, (nh, c, LANE), 2)
        dgb_ref[...] = jnp.where(lane == head, dbeta, jnp.where(lane == head + nh, dg, 0.0))

    blk = pl.BlockSpec((c, WIDTH_A), lambda i: (n - 1 - i, 0))
    mat = pl.BlockSpec((nh, 1, c, c), lambda i: (0, n - 1 - i, 0, 0))
    return _call(
        body, name=name, grid=(n,),
        in_specs=[blk, blk, blk, pl.BlockSpec((c, LANE), lambda i: (n - 1 - i, 0)), mat, mat, blk],
        out_specs=[blk, blk, blk, pl.BlockSpec((nh, c, LANE), lambda i: (0, n - 1 - i, 0))],
        out_shape=[jax.ShapeDtypeStruct((t, WIDTH_A), F32)] * 3 + [jax.ShapeDtypeStruct((nh, t, LANE), F32)],
        scratch_shapes=[pltpu.VMEM((nh, HEAD_A, HEAD_A), F32)], sem=("arbitrary",),
        args=(q, k, v, gates, tm_all, s_all, do), comm=comm)


def _onorm_fwd(o, gate, g, name):
    t = o.shape[0]

    def body(o_ref, gate_ref, g_ref, y_ref):
        ov, gv = o_ref[...], gate_ref[...]
        r = lax.rsqrt(jnp.mean(ov * ov, axis=-1, keepdims=True) + RMS_EPS)
        y_ref[...] = (ov * r * g_ref[...] * gv * _sigmoid(gv)).astype(BF16)

    blk = pl.BlockSpec((t, HEAD_A), lambda j: (0, j))
    return pl.pallas_call(
        body, name=name, grid=(N_HEADS_A,), in_specs=[blk, blk, pl.BlockSpec((1, HEAD_A), lambda j: (0, 0))],
        out_specs=blk, out_shape=jax.ShapeDtypeStruct((t, WIDTH_A), BF16),
        compiler_params=_params(("parallel",)))(o, gate, g)


def _onorm_bwd(o, gate, g, dy, name):
    t = o.shape[0]

    def body(o_ref, gate_ref, g_ref, dy_ref, do_ref, dgate_ref, dg_ref):
        @pl.when(pl.program_id(0) == 0)
        def _():
            dg_ref[...] = jnp.zeros_like(dg_ref)

        ov, gv, dyv = o_ref[...], gate_ref[...], dy_ref[...].astype(F32)
        r = lax.rsqrt(jnp.mean(ov * ov, axis=-1, keepdims=True) + RMS_EPS)
        oh = ov * r
        sg, dsg = _silu_and_grad(gv)
        dgate_ref[...] = (dyv * oh * g_ref[...] * dsg).astype(BF16)
        dn = dyv * sg
        dg_ref[...] += jnp.sum(dn * oh, axis=0, keepdims=True)
        dng = dn * g_ref[...]
        do_ref[...] = r * (dng - oh * jnp.mean(dng * oh, axis=-1, keepdims=True))

    blk = pl.BlockSpec((t, HEAD_A), lambda j: (0, j))
    vec = pl.BlockSpec((1, HEAD_A), lambda j: (0, 0))
    return pl.pallas_call(
        body, name=name, grid=(N_HEADS_A,), in_specs=[blk, blk, vec, blk], out_specs=[blk, blk, vec],
        out_shape=[jax.ShapeDtypeStruct((t, WIDTH_A), F32), jax.ShapeDtypeStruct((t, WIDTH_A), BF16),
                   jax.ShapeDtypeStruct((1, HEAD_A), F32)],
        compiler_params=_params(("arbitrary",)))(o, gate, g, dy)


def _cmul(ar, ai, br, bi):
    return ar * br - ai * bi, ar * bi + ai * br


def _scan_tables(ar, ai, reverse):
    p1 = (ar, ai)
    p2 = _cmul(*p1, *p1)
    p4 = _cmul(*p2, *p2)
    p8 = _cmul(*p4, *p4)
    p3 = _cmul(*p2, *p1)
    p5 = _cmul(*p4, *p1)
    p6 = _cmul(*p4, *p2)
    p7 = _cmul(*p4, *p3)
    pows = [p1, p2, p3, p4, p5, p6, p7, p8]
    rows = lax.broadcasted_iota(jnp.int32, (8, ar.shape[1]), 0)
    tr = jnp.zeros((8, ar.shape[1]), F32)
    ti = jnp.zeros((8, ar.shape[1]), F32)
    for r in range(8):
        pw = pows[7 - r] if reverse else pows[r]
        tr = jnp.where(rows == r, pw[0], tr)
        ti = jnp.where(rows == r, pw[1], ti)
    return p1, p2, p4, p8, tr, ti


def _tile_scan(xr, xi, p1, p2, p4, reverse):
    rows = lax.broadcasted_iota(jnp.int32, xr.shape, 0)
    for s, (pr, pi) in ((1, p1), (2, p2), (4, p4)):
        if reverse:
            keep = rows < 8 - s
            sr, si = pltpu.roll(xr, 8 - s, 0), pltpu.roll(xi, 8 - s, 0)
        else:
            keep = rows >= s
            sr, si = pltpu.roll(xr, s, 0), pltpu.roll(xi, s, 0)
        sr, si = jnp.where(keep, sr, 0.0), jnp.where(keep, si, 0.0)
        mr, mi = _cmul(pr, pi, sr, si)
        xr, xi = xr + mr, xi + mi
    return xr, xi


def _s5_scan_fwd(bu, a, name, tb=512, comm=None):
    t = bu.shape[0]
    cb = SCAN_CB
    nt = t // tb

    def body(b_ref, a_ref, x_ref, carry):
        @pl.when(pl.program_id(1) == 0)
        def _():
            carry[...] = jnp.zeros_like(carry)

        ar, ai = a_ref[:, 0:cb], a_ref[:, cb:2 * cb]
        p1, p2, p4, p8, tr, ti = _scan_tables(ar, ai, False)

        def step(j, c):
            cr, ci = c
            i = pl.multiple_of(j * 8, 8)
            xr, xi = _tile_scan(b_ref[pl.ds(i, 8), 0:cb], b_ref[pl.ds(i, 8), cb:2 * cb], p1, p2, p4, False)
            mr, mi = _cmul(tr, ti, cr, ci)
            xr, xi = xr + mr, xi + mi
            x_ref[pl.ds(i, 8), 0:cb] = xr
            x_ref[pl.ds(i, 8), cb:2 * cb] = xi
            return xr[7:8, :], xi[7:8, :]

        cr, ci = lax.fori_loop(0, tb // 8, step, (carry[0:1, :], carry[1:2, :]), unroll=2)
        carry[0:1, :] = cr
        carry[1:2, :] = ci

    blk = pl.BlockSpec((tb, 2 * cb), lambda j, i: (i, j))
    return _call(
        body, name=name, grid=(SSM_CH // cb, nt),
        in_specs=[blk, pl.BlockSpec((1, 2 * cb), lambda j, i: (0, j))], out_specs=blk,
        out_shape=jax.ShapeDtypeStruct((t, 2 * SSM_CH), F32), scratch_shapes=[pltpu.VMEM((8, cb), F32)],
        sem=("parallel", "arbitrary"), args=(bu, a), comm=comm)


def _s5_scan_bwd(dx, x, a, name, tb=512, comm=None):
    t = dx.shape[0]
    cb = SCAN_CB
    nt = t // tb
    nj = tb // 8

    def body(d_ref, x_ref, xp_ref, a_ref, l_ref, da_ref, carry, acc):
        tblk = pl.program_id(1)

        @pl.when(tblk == 0)
        def _():
            carry[...] = jnp.zeros_like(carry)
            acc[...] = jnp.zeros_like(acc)

        ar, ai = a_ref[:, 0:cb], a_ref[:, cb:2 * cb]
        p1, p2, p4, p8, tr, ti = _scan_tables(ar, -ai, True)
        rows = lax.broadcasted_iota(jnp.int32, (8, cb), 0)

        def step(jj, c):
            cr, ci, sr_acc, si_acc = c
            j = nj - 1 - jj
            i = pl.multiple_of(j * 8, 8)
            lr, li = _tile_scan(d_ref[pl.ds(i, 8), 0:cb], d_ref[pl.ds(i, 8), cb:2 * cb], p1, p2, p4, True)
            mr, mi = _cmul(tr, ti, cr, ci)
            lr, li = lr + mr, li + mi
            l_ref[pl.ds(i, 8), 0:cb] = lr
            l_ref[pl.ds(i, 8), cb:2 * cb] = li
            ip = pl.multiple_of(jnp.maximum(j - 1, 0) * 8, 8)
            prev_r = jnp.where(j > 0, x_ref[pl.ds(ip, 8), 0:cb], xp_ref[:, 0:cb])
            prev_i = jnp.where(j > 0, x_ref[pl.ds(ip, 8), cb:2 * cb], xp_ref[:, cb:2 * cb])
            edge = jnp.where(jnp.logical_and(j == 0, tblk == nt - 1), 0.0, 1.0)
            xs_r = jnp.where(rows == 0, pltpu.roll(prev_r, 1, 0) * edge, pltpu.roll(x_ref[pl.ds(i, 8), 0:cb], 1, 0))
            xs_i = jnp.where(rows == 0, pltpu.roll(prev_i, 1, 0) * edge, pltpu.roll(x_ref[pl.ds(i, 8), cb:2 * cb], 1, 0))
            sr_acc = sr_acc + lr * xs_r + li * xs_i
            si_acc = si_acc + li * xs_r - lr * xs_i
            return lr[0:1, :], li[0:1, :], sr_acc, si_acc

        cr, ci, sr_acc, si_acc = lax.fori_loop(
            0, nj, step, (carry[0:1, :], carry[1:2, :], acc[:, 0:cb], acc[:, cb:2 * cb]))
        carry[0:1, :] = cr
        carry[1:2, :] = ci
        acc[:, 0:cb] = sr_acc
        acc[:, cb:2 * cb] = si_acc

        @pl.when(tblk == nt - 1)
        def _():
            da_ref[...] = jnp.sum(acc[...], axis=0, keepdims=True)

    blk = pl.BlockSpec((tb, 2 * cb), lambda j, i: (nt - 1 - i, j))
    prev = pl.BlockSpec((8, 2 * cb), lambda j, i: (jnp.maximum((nt - 1 - i) * (tb // 8) - 1, 0), j))
    vec = pl.BlockSpec((1, 2 * cb), lambda j, i: (0, j))
    return _call(
        body, name=name, grid=(SSM_CH // cb, nt), in_specs=[blk, blk, prev, vec], out_specs=[blk, vec],
        out_shape=[jax.ShapeDtypeStruct((t, 2 * SSM_CH), F32), jax.ShapeDtypeStruct((1, 2 * SSM_CH), F32)],
        scratch_shapes=[pltpu.VMEM((8, cb), F32), pltpu.VMEM((8, 2 * cb), F32)],
        sem=("parallel", "arbitrary"), args=(dx, x, x, a), comm=comm)


def _glu_fwd(yc, u, dvec, wg, bg, name, tr=256):
    t = yc.shape[0]

    def body(yc_ref, u_ref, d_ref, w_ref, b_ref, yl_ref, yb_ref):
        yl = yc_ref[...] + d_ref[...] * u_ref[...]
        yl_ref[...] = yl
        yg, _ = _gelu_and_grad(yl)
        z = jnp.dot(yg.astype(BF16), w_ref[...], preferred_element_type=F32) + b_ref[...]
        yb_ref[...] = (yg * _sigmoid(z)).astype(BF16)

    blk = pl.BlockSpec((tr, SSM_WIDTH), lambda i: (i, 0))
    vec = pl.BlockSpec((1, SSM_WIDTH), lambda i: (0, 0))
    return pl.pallas_call(
        body, name=name, grid=(t // tr,),
        in_specs=[blk, blk, vec, pl.BlockSpec((SSM_WIDTH, SSM_WIDTH), lambda i: (0, 0)), vec],
        out_specs=[blk, blk],
        out_shape=[jax.ShapeDtypeStruct((t, SSM_WIDTH), F32), jax.ShapeDtypeStruct((t, SSM_WIDTH), BF16)],
        compiler_params=_params(("parallel",)))(yc, u, dvec, wg, bg)


def _glu_bwd(yl, u, dvec, wg, bg, dyb, name, tr=256):
    t = yl.shape[0]

    def body(yl_ref, u_ref, d_ref, w_ref, b_ref, dy_ref, dyl_ref, du_ref, dw_ref, db_ref, dd_ref):
        @pl.when(pl.program_id(0) == 0)
        def _():
            dw_ref[...] = jnp.zeros_like(dw_ref)
            db_ref[...] = jnp.zeros_like(db_ref)
            dd_ref[...] = jnp.zeros_like(dd_ref)

        ylv, dyv, wv = yl_ref[...], dy_ref[...].astype(F32), w_ref[...]
        yg, dgelu = _gelu_and_grad(ylv)
        ygb = yg.astype(BF16)
        z = jnp.dot(ygb, wv, preferred_element_type=F32) + b_ref[...]
        sg = _sigmoid(z)
        dz = dyv * yg * sg * (1.0 - sg)
        dzb = dz.astype(BF16)
        dyg = dyv * sg + lax.dot_general(dzb, wv, (((1,), (1,)), ((), ())), preferred_element_type=F32)
        dyl = dyg * dgelu
        dyl_ref[...] = dyl.astype(BF16)
        du_ref[...] = dyl * d_ref[...]
        dw_ref[...] += lax.dot_general(ygb, dzb, (((0,), (0,)), ((), ())), preferred_element_type=F32)
        db_ref[...] += jnp.sum(dz, axis=0, keepdims=True)
        dd_ref[...] += jnp.sum(dyl * u_ref[...], axis=0, keepdims=True)

    blk = pl.BlockSpec((tr, SSM_WIDTH), lambda i: (i, 0))
    vec = pl.BlockSpec((1, SSM_WIDTH), lambda i: (0, 0))
    wsp = pl.BlockSpec((SSM_WIDTH, SSM_WIDTH), lambda i: (0, 0))
    return pl.pallas_call(
        body, name=name, grid=(t // tr,), in_specs=[blk, blk, vec, wsp, vec, blk],
        out_specs=[blk, blk, wsp, vec, vec],
        out_shape=[jax.ShapeDtypeStruct((t, SSM_WIDTH), BF16), jax.ShapeDtypeStruct((t, SSM_WIDTH), F32),
                   jax.ShapeDtypeStruct((SSM_WIDTH, SSM_WIDTH), F32), jax.ShapeDtypeStruct((1, SSM_WIDTH), F32),
                   jax.ShapeDtypeStruct((1, SSM_WIDTH), F32)],
        compiler_params=_params(("arbitrary",)))(yl, u, dvec, wg, bg, dyb)


def _mesh_pos():
    return lax.axis_index("x"), lax.axis_index("y"), lax.axis_index("c")


def _device_index():
    x, y, c = _mesh_pos()
    return 4 * x + 2 * y + c


def _gather_comm(arrays):
    na = len(arrays)

    def own_copy(ins, outs, sems, ai):
        return pltpu.make_async_copy(ins[ai], outs[ai].at[_device_index()], sems[2].at[ai])

    def ctx(ins, outs, sems):
        send_sems, recv_sems = sems[:2]
        x, y, c = _mesh_pos()
        chips = [(1 - x, y), (x, 1 - y), (1 - x, 1 - y)]

        def copy(ai, kk, block, to, own=False):
            slot = outs[ai].at[4 * block[0] + 2 * block[1] + block[2]]
            return pltpu.make_async_remote_copy(
                src_ref=ins[ai] if own else slot, dst_ref=slot, send_sem=send_sems.at[ai, kk],
                recv_sem=recv_sems.at[ai, kk], device_id=to, device_id_type=MESH)

        return (x, y, c), (x, y, 1 - c), chips, c, copy

    def start(ins, outs, sems):
        me, sibling, chips, c, copy = ctx(ins, outs, sems)
        for ai in range(na):
            copy(ai, 0, me, sibling, own=True).start()
            for j, chip in enumerate(chips):
                copy(ai, 1 + j, me, (*chip, c), own=True).start()
        for ai in range(na):
            own_copy(ins, outs, sems, ai).start()

    def mid(ins, outs, sems):
        me, sibling, chips, c, copy = ctx(ins, outs, sems)
        for ai in range(na):
            for j, chip in enumerate(chips):
                copy(ai, 1 + j, (*chip, c), me).wait_recv()
                copy(ai, 4 + j, (*chip, c), sibling).start()

    def end(ins, outs, sems):
        me, sibling, chips, c, copy = ctx(ins, outs, sems)
        for ai in range(na):
            copy(ai, 0, sibling, me).wait_recv()
            copy(ai, 0, me, sibling, own=True).wait_send()
            for j, chip in enumerate(chips):
                copy(ai, 4 + j, (*chip, 1 - c), me).wait_recv()
                copy(ai, 1 + j, me, (*chip, c), own=True).wait_send()
                copy(ai, 4 + j, (*chip, c), sibling).wait_send()
            own_copy(ins, outs, sems, ai).wait()

    return Comm(arrays, [jax.ShapeDtypeStruct((N_DEV,) + a.shape, a.dtype) for a in arrays],
                [pltpu.SemaphoreType.DMA((na, 7)), pltpu.SemaphoreType.DMA((na, 7)), pltpu.SemaphoreType.DMA((na,))],
                start, end, mid)


def _sequencer_gather(arrays, name, collective_id):
    comm = _gather_comm(arrays)
    na = len(arrays)

    def body(*refs):
        ins, outs, sems = refs[:na], refs[na:2 * na], refs[2 * na:]
        x, y, c = _mesh_pos()
        peers = [(x, y, 1 - c), (1 - x, y, c), (x, 1 - y, c), (1 - x, 1 - y, c)]
        barrier = pltpu.get_barrier_semaphore()
        for peer in peers:
            pl.semaphore_signal(barrier, inc=1, device_id=peer, device_id_type=MESH)
        pl.semaphore_wait(barrier, len(peers))
        comm.start(ins, outs, sems)
        comm.mid(ins, outs, sems)
        comm.end(ins, outs, sems)

    return list(pl.kernel(
        body, out_type=tuple(comm.out_shapes), mesh=plsc.ScalarSubcoreMesh(axis_name="sequencer", num_cores=1),
        name=name, scratch_types=tuple(comm.sems),
        compiler_params=pltpu.CompilerParams(collective_id=collective_id))(*arrays))


def _sequencer_exchange(comm, peers_of, name, collective_id):
    na = len(comm.inputs)

    def body(*refs):
        ins, outs, sems = refs[:na], refs[na:na + len(comm.out_shapes)], refs[na + len(comm.out_shapes):]
        peers = peers_of(*_mesh_pos())
        barrier = pltpu.get_barrier_semaphore()
        for peer in peers:
            pl.semaphore_signal(barrier, inc=1, device_id=peer, device_id_type=MESH)
        pl.semaphore_wait(barrier, len(peers))
        comm.start(ins, outs, sems)
        comm.end(ins, outs, sems)

    return list(pl.kernel(
        body, out_type=tuple(comm.out_shapes), mesh=plsc.ScalarSubcoreMesh(axis_name="sequencer", num_cores=1),
        name=name, scratch_types=tuple(comm.sems),
        compiler_params=pltpu.CompilerParams(collective_id=collective_id))(*comm.inputs))


SIBLING_SWAP_ID, CHIP_EXCHANGE_ID = 9, 10


def _sequencer_swap(arrays, name):
    return _sequencer_exchange(_swap_comm(arrays), lambda x, y, c: [(x, y, 1 - c)], name, SIBLING_SWAP_ID)[0]


def _sequencer_chips(send, name):
    return _sequencer_exchange(_chips_comm(send), lambda x, y, c: [(1 - x, y, c), (x, 1 - y, c), (1 - x, 1 - y, c)],
                               name, CHIP_EXCHANGE_ID)[0]


def _swap_comm(arrays):
    na = len(arrays)
    offs = np.concatenate([[0], np.cumsum([a.shape[1] for a in arrays])]).astype(int)

    def copies(ins, outs, sems):
        x, y, c = _mesh_pos()
        return [pltpu.make_async_remote_copy(
            src_ref=ins[ai].at[2 * k + 1 - c], dst_ref=outs[0].at[k, pl.ds(int(offs[ai]), arrays[ai].shape[1])],
            send_sem=sems[0].at[ai, k], recv_sem=sems[1].at[ai, k], device_id=(x, y, 1 - c), device_id_type=MESH)
            for ai in range(na) for k in range(4)]

    def start(ins, outs, sems):
        for cp in copies(ins, outs, sems):
            cp.start()

    def end(ins, outs, sems):
        for cp in copies(ins, outs, sems):
            cp.wait()

    return Comm(arrays, [jax.ShapeDtypeStruct((4, int(offs[-1]), PACK_COLS), arrays[0].dtype)],
                [pltpu.SemaphoreType.DMA((na, 4)), pltpu.SemaphoreType.DMA((na, 4))], start, end)


def _chips_comm(send):
    def copies(ins, outs, sems):
        x, y, c = _mesh_pos()
        chips = [(1 - x, y), (x, 1 - y), (1 - x, 1 - y)]
        return [pltpu.make_async_remote_copy(
            src_ref=ins[0].at[2 * cx + cy], dst_ref=outs[0].at[j], send_sem=sems[0].at[j], recv_sem=sems[1].at[j],
            device_id=(cx, cy, c), device_id_type=MESH) for j, (cx, cy) in enumerate(chips)]

    def start(ins, outs, sems):
        for cp in copies(ins, outs, sems):
            cp.start()

    def end(ins, outs, sems):
        for cp in copies(ins, outs, sems):
            cp.wait()

    return Comm([send], [jax.ShapeDtypeStruct((3,) + send.shape[1:], send.dtype)],
                [pltpu.SemaphoreType.DMA((3,)), pltpu.SemaphoreType.DMA((3,))], start, end)


def _all_gather(arrays, name):
    na = len(arrays)

    def body(*refs):
        ins, outs = refs[:na], refs[na:2 * na]
        send_sems, recv_sems, local_sems = refs[2 * na:]
        x, y, c = _mesh_pos()
        me, sibling = (x, y, c), (x, y, 1 - c)
        chips = [(1 - x, y), (x, 1 - y), (1 - x, 1 - y)]
        waits = []
        for ai in range(na):
            in_ref, out_ref = ins[ai], outs[ai]

            def slot(px, py, pc, out_ref=out_ref):
                return out_ref.at[4 * px + 2 * py + pc]

            def copy(kk, block, to, src=None, ai=ai, slot=slot):
                return pltpu.make_async_remote_copy(
                    src_ref=slot(*block) if src is None else src, dst_ref=slot(*block),
                    send_sem=send_sems.at[ai, kk], recv_sem=recv_sems.at[ai, kk], device_id=to, device_id_type=MESH)

            mine = pltpu.make_async_copy(in_ref, slot(*me), local_sems.at[ai])
            mine.start()
            first = [copy(0, me, sibling, src=in_ref)]
            first += [copy(1 + j, me, (*chip, c), src=in_ref) for j, chip in enumerate(chips)]
            for cp in first:
                cp.start()
            waits.append((copy, mine, first))
        sends = []
        for ai in range(na):
            copy, mine, first = waits[ai]
            passed = [copy(4 + j, (*chip, c), sibling) for j, chip in enumerate(chips)]
            for j, chip in enumerate(chips):
                copy(1 + j, (*chip, c), me).wait_recv()
                passed[j].start()
            sends.append(passed)
        for ai in range(na):
            copy, mine, first = waits[ai]
            copy(0, sibling, me).wait_recv()
            for j, chip in enumerate(chips):
                copy(4 + j, (*chip, 1 - c), me).wait_recv()
            for cp in first + sends[ai]:
                cp.wait_send()
            mine.wait()

    any_spec = pl.BlockSpec(memory_space=pl.ANY)
    return pl.pallas_call(
        body, name=name, in_specs=[any_spec] * na, out_specs=[any_spec] * na,
        out_shape=[jax.ShapeDtypeStruct((N_DEV,) + a.shape, a.dtype) for a in arrays],
        scratch_shapes=[pltpu.SemaphoreType.DMA((na, 7)), pltpu.SemaphoreType.DMA((na, 7)),
                        pltpu.SemaphoreType.DMA((na,))],
        compiler_params=pltpu.CompilerParams(has_side_effects=True))(*arrays)


def _swap_sibling(arrays, name):
    na = len(arrays)
    offs = np.concatenate([[0], np.cumsum([a.shape[1] for a in arrays])]).astype(int)
    rows = int(offs[-1])

    def body(*refs):
        ins, recv_ref = refs[:na], refs[na]
        send_sems, recv_sems = refs[na + 1:]
        x, y, c = _mesh_pos()
        started = []
        for ai in range(na):
            span = pl.ds(int(offs[ai]), arrays[ai].shape[1])
            for k in range(4):
                remote = pltpu.make_async_remote_copy(
                    src_ref=ins[ai].at[2 * k + 1 - c], dst_ref=recv_ref.at[k, span], send_sem=send_sems.at[ai, k],
                    recv_sem=recv_sems.at[ai, k], device_id=(x, y, 1 - c), device_id_type=MESH)
                remote.start()
                started.append(remote)
        for remote in started:
            remote.wait()

    any_spec = pl.BlockSpec(memory_space=pl.ANY)
    return pl.pallas_call(
        body, name=name, in_specs=[any_spec] * na, out_specs=any_spec,
        out_shape=jax.ShapeDtypeStruct((4, rows, PACK_COLS), arrays[0].dtype),
        scratch_shapes=[pltpu.SemaphoreType.DMA((na, 4)), pltpu.SemaphoreType.DMA((na, 4))])(*arrays)


def _exchange_chips(send, name):
    def body(s_ref, o_ref, send_sems, recv_sems):
        x, y, c = _mesh_pos()
        chips = [(1 - x, y), (x, 1 - y), (1 - x, 1 - y)]
        cps = [pltpu.make_async_remote_copy(
            src_ref=s_ref.at[2 * cx + cy], dst_ref=o_ref.at[j], send_sem=send_sems.at[j], recv_sem=recv_sems.at[j],
            device_id=(cx, cy, c), device_id_type=MESH) for j, (cx, cy) in enumerate(chips)]
        for cp in cps:
            cp.start()
        for cp in cps:
            cp.wait()

    any_spec = pl.BlockSpec(memory_space=pl.ANY)
    return pl.pallas_call(
        body, name=name, in_specs=[any_spec], out_specs=any_spec,
        out_shape=jax.ShapeDtypeStruct((3,) + send.shape[1:], send.dtype),
        scratch_shapes=[pltpu.SemaphoreType.DMA((3,)), pltpu.SemaphoreType.DMA((3,))])(send)


def _pair_sum(keep, recv, name, tr=464):
    nchip, rows, cols = keep.shape

    def body(g_ref, r_ref, o_ref):
        o_ref[...] = (g_ref[...].astype(F32) + r_ref[...].astype(F32)).astype(BF16)

    blk = pl.BlockSpec((1, tr, cols), lambda k, i: (k, i, 0))
    return pl.pallas_call(
        body, name=name, grid=(nchip, rows // tr), in_specs=[blk, blk], out_specs=blk,
        out_shape=jax.ShapeDtypeStruct((nchip, rows, cols), BF16),
        compiler_params=_params(("parallel", "parallel")))(keep, recv)


def _pair_sum_pieces(pieces, recv, name, tr):
    _, rows, cols = pieces.shape
    core = lax.axis_index("c").astype(jnp.int32).reshape(1)

    def body(c_ref, g_ref, r_ref, o_ref):
        del c_ref
        o_ref[...] = (g_ref[...].astype(F32) + r_ref[...].astype(F32)).astype(BF16)

    grid_spec = pltpu.PrefetchScalarGridSpec(
        num_scalar_prefetch=1, grid=(4, rows // tr),
        in_specs=[pl.BlockSpec((1, tr, cols), lambda k, i, c_ref: (2 * k + c_ref[0], i, 0)),
                  pl.BlockSpec((1, tr, cols), lambda k, i, c_ref: (k, i, 0))],
        out_specs=pl.BlockSpec((1, tr, cols), lambda k, i, c_ref: (k, i, 0)))
    return pl.pallas_call(
        body, name=name, grid_spec=grid_spec, out_shape=jax.ShapeDtypeStruct((4, rows, cols), BF16),
        compiler_params=_params(("parallel", "parallel")))(core, pieces, recv)


def _chip_sum(own, others, name, tr=464):
    _, rows, cols = own.shape
    chip = (2 * lax.axis_index("x") + lax.axis_index("y")).astype(jnp.int32).reshape(1)

    def body(chip_ref, own_ref, oth_ref, o_ref):
        del chip_ref
        acc = own_ref[0].astype(F32)
        for j in range(3):
            acc = acc + oth_ref[j].astype(F32)
        o_ref[...] = acc

    grid_spec = pltpu.PrefetchScalarGridSpec(
        num_scalar_prefetch=1, grid=(rows // tr,),
        in_specs=[pl.BlockSpec((1, tr, cols), lambda i, chip_ref: (chip_ref[0], i, 0)),
                  pl.BlockSpec((3, tr, cols), lambda i, chip_ref: (0, i, 0))],
        out_specs=pl.BlockSpec((tr, cols), lambda i, chip_ref: (i, 0)))
    return pl.pallas_call(
        body, name=name, grid_spec=grid_spec, out_shape=jax.ShapeDtypeStruct((rows, cols), F32),
        compiler_params=_params(("parallel",)))(chip, own, others)


def _sum_leading(parts, name, tr=464):
    nparts, rows, cols = parts.shape
    tr = tr if rows % tr == 0 else rows

    def body(p_ref, o_ref):
        acc = p_ref[0].astype(F32)
        for i in range(1, nparts):
            acc = acc + p_ref[i].astype(F32)
        o_ref[...] = acc

    return pl.pallas_call(
        body, name=name, grid=(rows // tr,),
        in_specs=[pl.BlockSpec((nparts, tr, cols), lambda i: (0, i, 0))],
        out_specs=pl.BlockSpec((tr, cols), lambda i: (i, 0)), out_shape=jax.ShapeDtypeStruct((rows, cols), F32),
        compiler_params=_params(("parallel",)))(parts)


def _adamw(w, g, m, v, name, comm=None):
    shape = w.shape
    cols = shape[-1]
    lead = shape[0] if len(shape) >= 3 else 1
    rows = int(np.prod(shape[:-1])) // lead if len(shape) > 1 else 1
    w2, g2, m2, v2 = (a.reshape(lead, rows, cols) for a in (w, g, m, v))
    tr = rows
    for cand in (512, 256, 128, 64, 32, 16, 8):
        if rows % cand == 0 and rows > cand:
            tr = cand
            break
    bc1, bc2 = 1.0 - ADAM_B1 ** ADAM_STEP, 1.0 - ADAM_B2 ** ADAM_STEP

    def body(w_ref, g_ref, m_ref, v_ref, d_ref, nm_ref, nv_ref):
        gv = g_ref[...]
        nm = ADAM_B1 * m_ref[...] + (1.0 - ADAM_B1) * gv
        nv = ADAM_B2 * v_ref[...] + (1.0 - ADAM_B2) * (gv * gv)
        nm_ref[...] = nm
        nv_ref[...] = nv
        d_ref[...] = -ADAM_LR * ((nm / bc1) / (jnp.sqrt(nv / bc2) + ADAM_EPS) + ADAM_WD * w_ref[...])

    blk = pl.BlockSpec((1, tr, cols), lambda l, i: (l, i, 0))
    res = _call(body, name=name, grid=(lead, rows // tr), in_specs=[blk] * 4, out_specs=[blk] * 3,
                out_shape=[jax.ShapeDtypeStruct((lead, rows, cols), F32)] * 3, sem=("parallel", "parallel"),
                args=(w2, g2, m2, v2), comm=comm)
    outs, couts = res if comm is not None else (res, None)
    outs = tuple(o.reshape(shape) for o in outs)
    return outs if comm is None else (outs, couts)


WEIGHT_NAMES = ['norm_mix_g', 'norm_xa_g', 'norm_ffn_g', 'norm_mem_g', 'norm_final_g', 'w_in_ab', 'conv_qkv_a',
                'a_log_a', 'dt_bias_a', 'onorm_g_a', 'ssm_lambda_re', 'ssm_lambda_im', 'ssm_b_re', 'ssm_b_im',
                'ssm_c_re', 'ssm_c_im', 'ssm_d', 'ssm_log_dt', 'w_glu_b', 'b_glu_b', 'w_out_ab', 'pool_w',
                'pool_scale', 'xa_wq', 'xa_wkv', 'xa_wo', 'ffn_w_up', 'ffn_conv', 'ffn_w_down']
BIG_SHARDED = {'w_in_ab': ((1, 1024, 2568), 2), 'w_glu_b': ((1, 512, 512), 1), 'w_out_ab': ((1, 1024, 1024), 1),
               'pool_w': ((1, 4, 256, 256), 2), 'xa_wq': ((2, 1024, 1024), 1), 'xa_wkv': ((2, 1024, 2048), 2),
               'xa_wo': ((2, 1024, 1024), 1), 'ffn_w_up': ((2, 1024, 5632), 2), 'ffn_w_down': ((2, 2816, 1024), 1)}
SMALL_SHARDED = {'conv_qkv_a': ((1, 4, 1536), 2), 'pool_scale': ((1, 1024), 1), 'ffn_conv': ((2, 3, 5632), 2)}
REPLICATED = {'norm_mix_g': (2, 1024), 'norm_xa_g': (2, 1024), 'norm_ffn_g': (2, 1024), 'norm_mem_g': (1024,),
              'norm_final_g': (1024,), 'a_log_a': (1, 4), 'dt_bias_a': (1, 4), 'onorm_g_a': (1, 128),
              'ssm_lambda_re': (1, 32, 64), 'ssm_lambda_im': (1, 32, 64), 'ssm_b_re': (1, 32, 64, 16),
              'ssm_b_im': (1, 32, 64, 16), 'ssm_c_re': (1, 32, 16, 64), 'ssm_c_im': (1, 32, 16, 64),
              'ssm_d': (1, 32, 16), 'ssm_log_dt': (1, 32), 'b_glu_b': (1, 512)}
PACK_ROW_ALIGN = 8


def _shard_shape(shape, axis):
    return tuple(s // N_DEV if i == axis else s for i, s in enumerate(shape))


def _round_up(n, m):
    return (n + m - 1) // m * m


def _pack(arrays):
    total = sum(int(np.prod(a.shape)) for a in arrays)
    padded = _round_up(total, PACK_COLS * PACK_ROW_ALIGN)
    parts = [a.astype(F32).reshape(-1) for a in arrays]
    if padded != total:
        parts.append(jnp.zeros((padded - total,), F32))
    return jnp.concatenate(parts).reshape(padded // PACK_COLS, PACK_COLS)


def _unpack(packed, shapes):
    flat, out, off = packed.reshape(-1), [], 0
    for shape in shapes:
        size = int(np.prod(shape))
        out.append(flat[off:off + size].reshape(shape))
        off += size
    return out


def _split_shards(full, axis):
    shape = full.shape
    s = shape[axis] // N_DEV
    a = full.reshape(shape[:axis] + (N_DEV, s) + shape[axis + 1:])
    return jnp.moveaxis(a, axis, 0).reshape(N_DEV, -1)


def _merge_shards(pieces, shape, axis):
    sh = _shard_shape(shape, axis)
    a = pieces.reshape((N_DEV,) + sh)
    a = jnp.moveaxis(a, 0, axis)
    return a.reshape(shape)


_SCAN_NB = SSM_CH // SCAN_CB


def _to_scan_layout(m, axis):
    shape = m.shape
    m = m.reshape(shape[:axis] + (2, _SCAN_NB, SCAN_CB) + shape[axis + 1:])
    return jnp.swapaxes(m, axis, axis + 1).reshape(shape)


def _from_scan_layout(m, axis):
    shape = m.shape
    m = m.reshape(shape[:axis] + (_SCAN_NB, 2, SCAN_CB) + shape[axis + 1:])
    return jnp.swapaxes(m, axis, axis + 1).reshape(shape)


def _s5_discretise(lam_re, lam_im, b_re, b_im, log_dt):
    dt = jnp.exp(log_dt)[:, None]
    mag = jnp.exp(lam_re * dt)
    ang = lam_im * dt
    lb_re, lb_im = mag * jnp.cos(ang), mag * jnp.sin(ang)
    den = lam_re * lam_re + lam_im * lam_im
    nr, ni = lb_re - 1.0, lb_im
    coef_re = (nr * lam_re + ni * lam_im) / den
    coef_im = (ni * lam_re - nr * lam_im) / den
    bb_re = coef_re[..., None] * b_re - coef_im[..., None] * b_im
    bb_im = coef_re[..., None] * b_im + coef_im[..., None] * b_re
    return lb_re, lb_im, bb_re, bb_im


_GROUPS_PER_BLOCK = N_GROUPS // _SCAN_NB
_U_BLOCK = _GROUPS_PER_BLOCK * SSM_GROUP


def _s5_matrices(lb_re, lb_im, bb_re, bb_im, c_re, c_im):
    eye = jnp.eye(_GROUPS_PER_BLOCK, dtype=F32)
    blocked = lambda m: m.reshape((_SCAN_NB, _GROUPS_PER_BLOCK) + m.shape[1:])
    bmat = lambda bb: jnp.einsum('jgph,gk->jghkp', blocked(bb), eye).reshape(_SCAN_NB, _U_BLOCK, SCAN_CB)
    cmat = lambda cc: jnp.einsum('jghp,gk->jkpgh', blocked(cc), eye).reshape(_SCAN_NB, SCAN_CB, _U_BLOCK)
    b_in = jnp.concatenate([bmat(bb_re), bmat(bb_im)], axis=2)
    c_out = jnp.concatenate([cmat(c_re), -cmat(c_im)], axis=1)
    a_row = _to_scan_layout(jnp.concatenate([lb_re.reshape(1, SSM_CH), lb_im.reshape(1, SSM_CH)], axis=1), 1)
    return b_in, c_out, a_row


def _s5_matrix_grads(db_in, dc_out, da_row):
    da_nat = _from_scan_layout(da_row, 1)
    eye = jnp.eye(_GROUPS_PER_BLOCK, dtype=F32)
    nb, gb = _SCAN_NB, _GROUPS_PER_BLOCK
    bgrad = lambda m: jnp.einsum('jghkp,gk->jgph', m.reshape(nb, gb, SSM_GROUP, gb, SSM_STATE), eye
                                 ).reshape(N_GROUPS, SSM_STATE, SSM_GROUP)
    cgrad = lambda m: jnp.einsum('jkpgh,gk->jghp', m.reshape(nb, gb, SSM_STATE, gb, SSM_GROUP), eye
                                 ).reshape(N_GROUPS, SSM_GROUP, SSM_STATE)
    dbb_re, dbb_im = bgrad(db_in[:, :, :SCAN_CB]), bgrad(db_in[:, :, SCAN_CB:])
    dc_re, dc_im = cgrad(dc_out[:, :SCAN_CB]), -cgrad(dc_out[:, SCAN_CB:])
    dlb_re = da_nat[0, :SSM_CH].reshape(N_GROUPS, SSM_STATE)
    dlb_im = da_nat[0, SSM_CH:].reshape(N_GROUPS, SSM_STATE)
    return dlb_re, dlb_im, dbb_re, dbb_im, dc_re, dc_im


def _as_pieces(a):
    return a.reshape(N_DEV, a.shape[0] // N_DEV, a.shape[1])


def _hybrid_fwd(xn, x, wts, p, weights, riders):
    sv = {}
    hq = _mm(xn, wts['w_qkv_t'], "nt", "l0_in_qkv")
    gate = _mm(xn, wts['w_gate_t'], "nt", "l0_in_gate")
    ba = _mm(xn, wts['w_ba_t'], "nt", "l0_in_ba")
    u = _mm(xn, wts['w_u_t'], "nt", "l0_in_u")
    conv = p['conv_qkv']
    q = _qkv_pre_fwd(hq, conv, 0, 4, True, HEAD_A ** -0.5, "l0_q_pre")
    k = _qkv_pre_fwd(hq, conv, 4, 4, True, 1.0, "l0_k_pre")
    v = _qkv_pre_fwd(hq, conv, 8, 4, False, 1.0, "l0_v_pre")
    gates = _gates_fwd(ba, p['arow'], p['brow'], "l0_gates")
    o, tm_all, s_all = riders.run("l0_gdr_fwd", _gdr_fwd, q, k, v, gates)
    wts['w_glu'], wts['w_out'] = weights.full['w_glu'], weights.full['w_out']
    y_a = _onorm_fwd(o, gate, p['onorm_g'], "l0_onorm")
    bu = riders.run("l0_s5_bu", _mm_bd, u, p['b_in'], "nn")
    xs = riders.run("l0_s5_scan", _s5_scan_fwd, bu, p['a_row'])
    weights.gather_by_sequencer(GATHER_LAYER1, xs, "gather_layer1", GATHER_LAYER1_ID)
    yc = riders.run("l0_s5_cx", _mm_bd, xs, p['c_out'], "nn")
    yl, y_b = _glu_fwd(yc, u, p['d_row'], wts['w_glu'], p['b_glu'], "l0_glu")
    mixed = jnp.concatenate([y_a, y_b], axis=1)
    x1 = _mm(mixed, wts['w_out'], "nn", "l0_out", res=x)
    sv.update(hq=hq, gate=gate, ba=ba, u=u, q=q, k=k, v=v, gb=gates, o=o, tm=tm_all, s=s_all, xs=xs, yl=yl, mixed=mixed)
    return x1, sv


def _hybrid_bwd(dx1, xn, wts, p, sv, riders):
    gr = {}
    dmixed = _mm(dx1, wts['w_out'], "nt", "l0_out_dx", out_dtype=BF16)
    riders.grad('w_out', _as_pieces(_mm(sv['mixed'], dx1, "tn", "l0_out_dw", out_dtype=BF16)))
    dya, dyb = dmixed[:, :WIDTH_A], dmixed[:, WIDTH_A:]
    dyl, du_direct, dw_glu, gr['b_glu_b'], dd = _glu_bwd(
        sv['yl'], sv['u'], p['d_row'], wts['w_glu'], p['b_glu'], dyb, "l0_glu_bwd")
    riders.grad('w_glu', dw_glu.astype(BF16).reshape(N_DEV, -1, PACK_COLS))
    dxs = riders.run("l0_s5_cx_dx", _mm_bd, dyl, p['c_out'], "nt")
    dc_out = _mm_bd(sv['xs'], dyl, "tn", "l0_s5_cx_dw")
    lam, da_row = riders.run("l0_s5_scan_bwd", _s5_scan_bwd, dxs, sv['xs'], p['a_row'])
    du = _mm_bd(lam, p['b_in'], "nt", "l0_s5_bu_dx", res=du_direct, out_dtype=BF16)
    db_in = _mm_bd(sv['u'], lam, "tn", "l0_s5_bu_dw")
    gr['s5'] = (db_in, dc_out, da_row, dd)
    do, dgate, gr['onorm_g_a'] = _onorm_bwd(sv['o'], sv['gate'], p['onorm_g'], dya, "l0_onorm_bwd")
    dq, dk, dv, dgb = riders.run("l0_gdr_bwd", _gdr_bwd, sv['q'], sv['k'], sv['v'], sv['gb'], sv['tm'], sv['s'], do)
    conv = p['conv_qkv']
    dhq_q, dcw_q = _qkv_pre_bwd(sv['hq'], conv, dq, 0, 4, True, HEAD_A ** -0.5, "l0_q_pre_bwd")
    dhq_k, dcw_k = _qkv_pre_bwd(sv['hq'], conv, dk, 4, 4, True, 1.0, "l0_k_pre_bwd")
    dhq_v, dcw_v = _qkv_pre_bwd(sv['hq'], conv, dv, 8, 4, False, 1.0, "l0_v_pre_bwd")
    gr['conv_qkv_a'] = jnp.concatenate([dcw_q, dcw_k, dcw_v], axis=1)
    dhq = jnp.concatenate([dhq_q, dhq_k, dhq_v], axis=1)
    dba, da_log, ddt_bias = _gates_bwd(sv['ba'], p['arow'], p['brow'], dgb, "l0_gates_bwd")
    gr['a_log_a'], gr['dt_bias_a'] = da_log[:, 4:8], ddt_bias[:, 4:8]
    dw_qkv_t = _mm(dhq, xn, "tn", "l0_in_qkv_dw", out_dtype=BF16)
    dw_gate_t = _mm(dgate, xn, "tn", "l0_in_gate_dw", out_dtype=BF16)
    dw_ba_t = _mm(dba, xn, "tn", "l0_in_ba_dw", out_dtype=BF16)
    dw_u_t = _mm(du, xn, "tn", "l0_in_u_dw", out_dtype=BF16)
    dw_in_t = _as_pieces(jnp.concatenate([dw_qkv_t, dw_gate_t, dw_ba_t[:8], dw_u_t], axis=0))
    riders.grad('w_in_t', jnp.concatenate(
        [dw_in_t, jnp.zeros((N_DEV, dict(PIECES)['w_in_t'] - W_IN_PIECE, D_MODEL), BF16)], axis=1))
    dxn = riders.run("l0_in_qkv_dx", _mm, dhq, wts['w_qkv_t'], "nn")
    dxn = _mm(dgate, wts['w_gate_t'], "nn", "l0_in_gate_dx", res=dxn)
    dxn = _mm(dba, wts['w_ba_t'], "nn", "l0_in_ba_dx", res=dxn)
    dxn = riders.run("l0_in_u_dx", _mm, du, wts['w_u_t'], "nn", res=dxn)
    return dxn, gr


def _xa_fwd(x1, g, mem_n, wq, wkv_t, wo, tag, riders):
    xq = _rms_fwd(x1, g, BF16, tag + "_norm")
    q = _mm(xq, wq, "nn", tag + "_q", out_dtype=BF16)
    kv = _mm(mem_n, wkv_t, "nt", tag + "_kv", out_dtype=BF16)
    o = riders.run(tag + "_attn", _attn_fwd, q, kv)
    x2 = _mm(o, wo, "nn", tag + "_o", res=x1)
    return x2, dict(xq=xq, q=q, kv=kv, o=o)


def _xa_bwd(dx2, x1, g, mem_n, wq, wkv_t, wo, sv, tag, layer, riders):
    do = _mm(dx2, wo, "nt", tag + "_o_dx", out_dtype=BF16)
    riders.grad('wo%d' % layer, _as_pieces(_mm(sv['o'], dx2, "tn", tag + "_o_dw", out_dtype=BF16)))
    dq, dk, dv = _attn_bwd(sv['q'], sv['kv'], do, tag + "_attn_bwd")
    dkv = jnp.concatenate([dk, dv], axis=1).astype(BF16)
    dxq = _mm(dq, wq, "nt", tag + "_q_dx")
    riders.grad('wq%d' % layer, _as_pieces(_mm(sv['xq'], dq, "tn", tag + "_q_dw", out_dtype=BF16)))
    dmem_n = _mm(dkv, wkv_t, "nn", tag + "_kv_dx")
    riders.grad('wkv_t%d' % layer, _as_pieces(_mm(dkv, mem_n, "tn", tag + "_kv_dw", out_dtype=BF16)))
    dx1, dg = riders.run(tag + "_norm_bwd", _rms_bwd, x1, g, dxq, dx2)
    return dx1, dmem_n, dg


def _ffn_fwd(x2, g, w_up_t, conv, w_down, tag, riders):
    xf = _rms_fwd(x2, g, BF16, tag + "_norm")
    h = riders.run(tag + "_up", _mm, xf, w_up_t, "nt")
    a = riders.run(tag + "_act", _ffn_act_fwd, h, conv)
    x3 = _mm(a, w_down, "nn", tag + "_down", res=x2)
    return x3, dict(xf=xf, h=h, a=a)


def _ffn_bwd(dx3, x2, g, w_up_t, conv, w_down, sv, tag, layer, riders):
    da = _mm(dx3, w_down, "nt", tag + "_down_dx")
    riders.grad('down%d' % layer, _as_pieces(_mm(sv['a'], dx3, "tn", tag + "_down_dw", out_dtype=BF16)))
    dh, dconv = riders.run(tag + "_act_bwd", _ffn_act_bwd, sv['h'], conv, da)
    dxf = riders.run(tag + "_up_dx", _mm, dh, w_up_t, "nn")
    dw_up_t = riders.run(tag + "_up_dw", _mm, dh, sv['xf'], "tn", out_dtype=BF16)
    riders.grad('up_t%d' % layer, _as_pieces(dw_up_t))
    dx2, dg = riders.run(tag + "_norm_bwd", _rms_bwd, x2, g, dxf, dx3)
    return dx2, dconv, dg


BIG_NAMES, SMALL_NAMES, REP_NAMES = list(BIG_SHARDED), list(SMALL_SHARDED), list(REPLICATED)
BIG_SIZES = [int(np.prod(_shard_shape(*BIG_SHARDED[n]))) for n in BIG_NAMES]
SMALL_SIZES = [int(np.prod(_shard_shape(*SMALL_SHARDED[n]))) for n in SMALL_NAMES]


PIECES = [('w_in_t', 384), ('w_glu', 32), ('w_out', 128), ('pool_w', 32), ('wq0', 128), ('wq1', 128),
          ('wkv_t0', 256), ('wkv_t1', 256), ('wo0', 128), ('wo1', 128), ('up_t0', 704), ('up_t1', 704),
          ('down0', 352), ('down1', 352)]
PIECE_OFFS = dict(zip([k for k, _ in PIECES], np.concatenate([[0], np.cumsum([r for _, r in PIECES])[:-1]]).tolist()))
W_IN_ROWS = 4 * WIDTH_A + 2 * N_HEADS_A + SSM_WIDTH
W_IN_PIECE = W_IN_ROWS // N_DEV


def _row_tile(rows):
    return max(t for t in range(16, min(rows, 512) + 1, 16) if rows % t == 0)


class _Riders:
    def __init__(self):
        self.waiting = {}
        self.deferred = {}
        self.grads = {}
        self.groups = []
        self.reduced = {}

    def add(self, host, comm, then):
        self.waiting.setdefault(host, []).append((comm, then))

    def after(self, marker, then):
        self.deferred.setdefault(marker, []).append(then)

    def mark(self, name, out=None):
        for cont in self.deferred.pop(name, []):
            step = cont()
            if step is not None:
                values, then = step
                out, values = lax.optimization_barrier((out, values))
                then(values)
        return out

    def run(self, name, fn, *args, **kw):
        riders = self.waiting.pop(name, [])
        if not riders:
            out = fn(*args, name=name, **kw)
        else:
            out, couts = fn(*args, name=name, comm=[c for c, _ in riders], **kw)
            for (_, then), got in zip(riders, couts):
                then(got)
        return self.mark(name, out)

    def grad(self, key, pieces):
        self.grads[key] = pieces
        for group in [g for g in self.groups if all(k in self.grads for k in g[1])]:
            self.groups.remove(group)
            self._reduce(*group)

    def _reduce(self, name, keys, pair_marker, sum_marker):
        arrays = [self.grads[k] for k in keys]
        rows = sum(a.shape[1] for a in arrays)
        tile = _row_tile(rows)
        from_sibling = _sequencer_swap(arrays, name + "_to_sibling")

        def after_swap():
            if len(arrays) == 1:
                chip_sums = _pair_sum_pieces(arrays[0], from_sibling, name + "_pair_sum", tr=tile)
            else:
                core = lax.axis_index("c")
                keep = jnp.concatenate(
                    [lax.dynamic_index_in_dim(a.reshape(4, 2, a.shape[1], PACK_COLS), core, 1, keepdims=False)
                     for a in arrays], axis=1)
                chip_sums = _pair_sum(keep, from_sibling, name + "_pair_sum", tr=tile)

            def exchange_among_chips(chip_sums):
                from_chips = _sequencer_chips(chip_sums, name + "_to_chips")

                def after_chips():
                    total = _chip_sum(chip_sums, from_chips, name + "_chip_sum", tr=tile)
                    off = 0
                    for k, a in zip(keys, arrays):
                        self.reduced[k] = total[off:off + a.shape[1]]
                        off += a.shape[1]

                self.after(sum_marker, after_chips)

            return chip_sums, exchange_among_chips

        self.after(pair_marker, after_swap)


class _Weights:
    def __init__(self, inp):
        bf = lambda a: a.astype(BF16)
        local = {'w_in_t': bf(inp['w_in_ab'][0]).T, 'w_glu': bf(inp['w_glu_b'][0]), 'w_out': bf(inp['w_out_ab'][0]),
                 'pool_w': bf(inp['pool_w'][0]),
                 'small': _pack([inp[n] for n in SMALL_NAMES])}
        for l in range(2):
            local['wq%d' % l] = bf(inp['xa_wq'][l])
            local['wkv_t%d' % l] = bf(inp['xa_wkv'][l]).T
            local['wo%d' % l] = bf(inp['xa_wo'][l])
            local['up_t%d' % l] = bf(inp['ffn_w_up'][l]).T
            local['down%d' % l] = bf(inp['ffn_w_down'][l])
        self.local, self.full = local, {}

    def plan(self, keys):
        return _gather_comm([self.local[k] for k in keys])

    def gather_by_sequencer(self, keys, after, name, collective_id):
        arrays = [self.local[k] for k in keys]
        tie = (after.reshape(-1)[0] * 0.0).astype(arrays[0].dtype)
        arrays[0] = arrays[0] + tie
        self.land(keys, _sequencer_gather(arrays, name, collective_id))

    def land(self, keys, gathered):
        for k, g in zip(keys, gathered):
            if k == 'small':
                off = 0
                for n, size in zip(SMALL_NAMES, SMALL_SIZES):
                    self.full[n] = _merge_shards(g.reshape(N_DEV, -1)[:, off:off + size], *SMALL_SHARDED[n])
                    off += size
            elif k == 'pool_w':
                self.full[k] = jnp.swapaxes(g, 0, 1).reshape(len(POOL_WINDOWS), POOL_GROUP, POOL_GROUP)
            else:
                self.full[k] = g.reshape(N_DEV * g.shape[1], g.shape[2])


GATHER_FIRST = ['w_in_t', 'small']
GATHER_RIDES = []
GATHER_LAYER0 = ['w_glu', 'w_out', 'wq0', 'wkv_t0', 'wo0', 'down0', 'up_t0']
GATHER_LAYER1 = ['pool_w', 'wq1', 'wkv_t1', 'wo1', 'up_t1', 'down1']
GATHER_LAYER0_ID, GATHER_LAYER1_ID = 7, 8
GRAD_RIDES = [('g_down1', ['down1'], 'l1_ffn_act_bwd', 'l1_xa_norm_bwd'),
              ('g_up1', ['up_t1'], 'l1_xa_norm_bwd', 'l0_ffn_up_dx'),
              ('g_xa1', ['wq1', 'wkv_t1', 'wo1', 'pool_w'], 'l0_ffn_act_bwd', 'l0_xa_norm_bwd'),
              ('g_down0', ['down0'], 'l0_ffn_up_dx', 'l0_s5_scan_bwd'),
              ('g_l0', ['up_t0', 'wq0', 'wkv_t0', 'wo0'], 'l0_s5_cx_dx', 'l0_in_qkv_dx'),
              ('g_out', ['w_out', 'w_glu'], 'l0_s5_scan_bwd', 'l0_in_qkv_dx'),
              ('g_in', ['w_in_t'], 'l0_in_u_dx', 'adamw_pool_w')]


def _local_step(inp):
    f32_of = lambda n: inp[n].astype(F32)
    weights = _Weights(inp)
    riders = _Riders()
    riders.groups = list(GRAD_RIDES)
    full = weights.full
    weights.land(GATHER_FIRST, _comm_only(weights.plan(GATHER_FIRST), "gather_first"))
    weights.gather_by_sequencer(GATHER_LAYER0, full['w_in_t'], "gather_layer0", GATHER_LAYER0_ID)
    for host, keys in GATHER_RIDES:
        riders.add(host, weights.plan(keys), functools.partial(weights.land, keys))
    w_in_t = full['w_in_t']
    wts0 = dict(w_qkv_t=w_in_t[:3 * WIDTH_A], w_gate_t=w_in_t[3 * WIDTH_A:4 * WIDTH_A],
                w_ba_t=jnp.concatenate([w_in_t[4 * WIDTH_A:4 * WIDTH_A + 8], jnp.zeros((LANE - 8, D_MODEL), BF16)], 0),
                w_u_t=w_in_t[4 * WIDTH_A + 8:])
    lb_disc, disc_vjp = jax.vjp(_s5_discretise, f32_of('ssm_lambda_re')[0], f32_of('ssm_lambda_im')[0],
                                f32_of('ssm_b_re')[0], f32_of('ssm_b_im')[0], f32_of('ssm_log_dt')[0])
    b_in, c_out, a_row = _s5_matrices(*lb_disc, f32_of('ssm_c_re')[0], f32_of('ssm_c_im')[0])
    zeros4 = jnp.zeros((1, 4), F32)
    p0 = dict(conv_qkv=full['conv_qkv_a'][0], onorm_g=f32_of('onorm_g_a'),
              arow=jnp.concatenate([zeros4, f32_of('a_log_a'), jnp.zeros((1, LANE - 8), F32)], 1),
              brow=jnp.concatenate([zeros4, f32_of('dt_bias_a'), jnp.zeros((1, LANE - 8), F32)], 1),
              b_in=b_in.astype(BF16), c_out=c_out.astype(BF16), a_row=a_row,
              d_row=f32_of('ssm_d').reshape(1, SSM_WIDTH), b_glu=f32_of('b_glu_b'))

    x0 = inp['x'][0]
    mem_n = _rms_fwd(inp['mem'][0], inp['norm_mem_g'], BF16, "mem_norm")
    xn0 = _rms_fwd(x0, inp['norm_mix_g'][0], BF16, "l0_mix_norm")
    x1, sv_mix0 = _hybrid_fwd(xn0, x0, wts0, p0, weights, riders)
    x2, sv_xa0 = _xa_fwd(x1, inp['norm_xa_g'][0], mem_n, full['wq0'], full['wkv_t0'], full['wo0'], "l0_xa", riders)
    x3, sv_ffn0 = _ffn_fwd(x2, inp['norm_ffn_g'][0], full['up_t0'], full['ffn_conv'][0], full['down0'], "l0_ffn", riders)
    xn1 = _rms_fwd(x3, inp['norm_mix_g'][1], F32, "l1_mix_norm")
    x4 = _pool_fwd(xn1, full['pool_w'], full['pool_scale'], x3, "l1_pool")
    x5, sv_xa1 = _xa_fwd(x4, inp['norm_xa_g'][1], mem_n, full['wq1'], full['wkv_t1'], full['wo1'], "l1_xa", riders)
    x6, sv_ffn1 = _ffn_fwd(x5, inp['norm_ffn_g'][1], full['up_t1'], full['ffn_conv'][1], full['down1'], "l1_ffn", riders)
    loss_part, dx6, dg_final = _loss_head(x6, inp['norm_final_g'], inp['loss_target'][0], "loss_head")

    dx5, dconv1, dg_ffn1 = _ffn_bwd(dx6, x5, inp['norm_ffn_g'][1], full['up_t1'], full['ffn_conv'][1], full['down1'],
                                    sv_ffn1, "l1_ffn", 1, riders)
    dx4, dmem1, dg_xa1 = _xa_bwd(dx5, x4, inp['norm_xa_g'][1], mem_n, full['wq1'], full['wkv_t1'], full['wo1'],
                                 sv_xa1, "l1_xa", 1, riders)
    dxn1, dpool_w, dpool_scale = _pool_bwd(xn1, full['pool_w'], full['pool_scale'], dx4, "l1_pool_bwd")
    pool_pieces = jnp.swapaxes(dpool_w.astype(BF16).reshape(len(POOL_WINDOWS), N_DEV, -1, POOL_GROUP), 0, 1)
    riders.grad('pool_w', pool_pieces.reshape(N_DEV, -1, PACK_COLS))
    dx3, dg_mix1 = riders.run("l1_mix_norm_bwd", _rms_bwd, x3, inp['norm_mix_g'][1], dxn1, dx4)
    dx2, dconv0, dg_ffn0 = _ffn_bwd(dx3, x2, inp['norm_ffn_g'][0], full['up_t0'], full['ffn_conv'][0], full['down0'],
                                    sv_ffn0, "l0_ffn", 0, riders)
    dx1, dmem0, dg_xa0 = _xa_bwd(dx2, x1, inp['norm_xa_g'][0], mem_n, full['wq0'], full['wkv_t0'], full['wo0'],
                                 sv_xa0, "l0_xa", 0, riders)
    dxn0, g_mix0 = _hybrid_bwd(dx1, xn0, wts0, p0, sv_mix0, riders)
    grad_x, dg_mix0 = _rms_bwd(x0, inp['norm_mix_g'][0], dxn0, dx1, "l0_mix_norm_bwd")
    _, dg_mem = _rms_bwd(inp['mem'][0], inp['norm_mem_g'], dmem0 + dmem1, None, "mem_norm_bwd")
    assert not riders.groups and not riders.waiting and all(k.startswith("adamw_") for k in riders.deferred), (
        riders.groups, list(riders.waiting), list(riders.deferred))

    db_in, dc_out, da_row, dd = g_mix0['s5']
    dlb_re, dlb_im, dbb_re, dbb_im, dc_re, dc_im = _s5_matrix_grads(db_in, dc_out, da_row)
    dlam_re, dlam_im, dbr, dbi, dlog_dt = disc_vjp((dlb_re, dlb_im, dbb_re, dbb_im))

    rep_grads = {
        'norm_mix_g': jnp.concatenate([dg_mix0, dg_mix1], 0), 'norm_xa_g': jnp.concatenate([dg_xa0, dg_xa1], 0),
        'norm_ffn_g': jnp.concatenate([dg_ffn0, dg_ffn1], 0), 'norm_mem_g': dg_mem.reshape(-1),
        'norm_final_g': dg_final.reshape(-1), 'a_log_a': g_mix0['a_log_a'], 'dt_bias_a': g_mix0['dt_bias_a'],
        'onorm_g_a': g_mix0['onorm_g_a'], 'ssm_lambda_re': dlam_re[None], 'ssm_lambda_im': dlam_im[None],
        'ssm_b_re': dbr[None], 'ssm_b_im': dbi[None], 'ssm_c_re': dc_re[None], 'ssm_c_im': dc_im[None],
        'ssm_d': dd.reshape(1, N_GROUPS, SSM_GROUP), 'ssm_log_dt': dlog_dt[None], 'b_glu_b': g_mix0['b_glu_b']}
    small_grads = {'conv_qkv_a': g_mix0['conv_qkv_a'][None], 'pool_scale': dpool_scale,
                   'ffn_conv': jnp.stack([dconv0, dconv1])}
    return loss_part, grad_x, riders, rep_grads, small_grads


ADAMW_ORDER = ['ffn_w_up', 'ffn_w_down', 'xa_wkv', 'xa_wq', 'xa_wo', 'w_out_ab', 'w_glu_b', 'pool_w', 'w_in_ab']


def _update(inp, loss_part, grad_x, riders, rep_grads, small_grads):
    dev = _device_index()
    misc_local = _pack([rep_grads[n] for n in REP_NAMES] + [small_grads[n] for n in SMALL_NAMES] + [loss_part])
    (misc_all,) = _sequencer_gather([misc_local], "gather_small_grads", GATHER_LAYER0_ID)
    piece = lambda key: riders.reduced[key]
    both = lambda name: jnp.stack([piece(name + '0'), piece(name + '1')])
    swap = lambda a: jnp.swapaxes(a, -1, -2)
    reduced = {'w_in_ab': lambda: piece('w_in_t')[:W_IN_PIECE][None],
               'w_glu_b': lambda: piece('w_glu').reshape(inp['w_glu_b'].shape),
               'w_out_ab': lambda: piece('w_out')[None], 'pool_w': lambda: piece('pool_w').reshape(inp['pool_w'].shape),
               'xa_wq': lambda: both('wq'), 'xa_wkv': lambda: both('wkv_t'), 'xa_wo': lambda: both('wo'),
               'ffn_w_up': lambda: both('up_t'), 'ffn_w_down': lambda: both('down')}
    transposed = ('w_in_ab', 'xa_wkv', 'ffn_w_up')
    grads, upd = {}, {}
    assert sorted(ADAMW_ORDER) == sorted(BIG_NAMES)
    for n in ADAMW_ORDER:
        fix = swap if n in transposed else (lambda a: a)
        g = reduced[n]()
        out = riders.run("adamw_" + n, _adamw, fix(inp[n]), g, fix(inp['m_' + n]), fix(inp['v_' + n]))
        upd[n], grads[n] = tuple(fix(o) for o in out), fix(g)
    assert not riders.waiting and not riders.deferred, (list(riders.waiting), list(riders.deferred))
    misc_sum = _sum_leading(misc_all, "small_grads_sum")
    misc = _unpack(misc_sum, [inp[n].shape for n in REP_NAMES] + [SMALL_SHARDED[n][0] for n in SMALL_NAMES] + [()])
    loss = misc.pop()
    for n, g in zip(REP_NAMES, misc):
        grads[n] = g
    for n, g in zip(SMALL_NAMES, misc[len(REP_NAMES):]):
        grads[n] = lax.dynamic_index_in_dim(_split_shards(g, SMALL_SHARDED[n][1]), dev, 0, keepdims=False
                                            ).reshape(inp[n].shape)
    tiny_names = REP_NAMES + SMALL_NAMES
    rep_total = sum(int(np.prod(inp[n].shape)) for n in REP_NAMES)
    packs = [_pack([inp[prefix + n] for n in tiny_names]) for prefix in ('', 'm_', 'v_')]
    g_pack = _pack([misc_sum.reshape(-1)[:rep_total]] + [grads[n] for n in SMALL_NAMES])
    tiny_out = [_unpack(o, [inp[n].shape for n in tiny_names])
                for o in _adamw(packs[0], g_pack, packs[1], packs[2], "adamw_small")]
    for i, n in enumerate(tiny_names):
        upd[n] = tuple(o[i] for o in tiny_out)

    outs = [loss, grad_x[None]]
    outs += [grads[n] for n in WEIGHT_NAMES]
    for i in range(3):
        outs += [upd[n][i] for n in WEIGHT_NAMES]
    return tuple(outs)


def _step(inp):
    loss_part, grad_x, riders, rep_grads, small_grads = _local_step(inp)
    return _update(inp, loss_part, grad_x, riders, rep_grads, small_grads)


INPUT_NAMES = (['x', 'mem'] + WEIGHT_NAMES + ['loss_target'] + ['m_' + n for n in WEIGHT_NAMES]
               + ['v_' + n for n in WEIGHT_NAMES])


def kernel(x, mem, norm_mix_g, norm_xa_g, norm_ffn_g, norm_mem_g, norm_final_g, w_in_ab, conv_qkv_a, a_log_a, dt_bias_a, onorm_g_a, ssm_lambda_re, ssm_lambda_im, ssm_b_re, ssm_b_im, ssm_c_re, ssm_c_im, ssm_d, ssm_log_dt, w_glu_b, b_glu_b, w_out_ab, pool_w, pool_scale, xa_wq, xa_wkv, xa_wo, ffn_w_up, ffn_conv, ffn_w_down, loss_target, m_norm_mix_g, m_norm_xa_g, m_norm_ffn_g, m_norm_mem_g, m_norm_final_g, m_w_in_ab, m_conv_qkv_a, m_a_log_a, m_dt_bias_a, m_onorm_g_a, m_ssm_lambda_re, m_ssm_lambda_im, m_ssm_b_re, m_ssm_b_im, m_ssm_c_re, m_ssm_c_im, m_ssm_d, m_ssm_log_dt, m_w_glu_b, m_b_glu_b, m_w_out_ab, m_pool_w, m_pool_scale, m_xa_wq, m_xa_wkv, m_xa_wo, m_ffn_w_up, m_ffn_conv, m_ffn_w_down, v_norm_mix_g, v_norm_xa_g, v_norm_ffn_g, v_norm_mem_g, v_norm_final_g, v_w_in_ab, v_conv_qkv_a, v_a_log_a, v_dt_bias_a, v_onorm_g_a, v_ssm_lambda_re, v_ssm_lambda_im, v_ssm_b_re, v_ssm_b_im, v_ssm_c_re, v_ssm_c_im, v_ssm_d, v_ssm_log_dt, v_w_glu_b, v_b_glu_b, v_w_out_ab, v_pool_w, v_pool_scale, v_xa_wq, v_xa_wkv, v_xa_wo, v_ffn_w_up, v_ffn_conv, v_ffn_w_down):
    args = (x, mem, norm_mix_g, norm_xa_g, norm_ffn_g, norm_mem_g, norm_final_g, w_in_ab, conv_qkv_a, a_log_a, dt_bias_a, onorm_g_a, ssm_lambda_re, ssm_lambda_im, ssm_b_re, ssm_b_im, ssm_c_re, ssm_c_im, ssm_d, ssm_log_dt, w_glu_b, b_glu_b, w_out_ab, pool_w, pool_scale, xa_wq, xa_wkv, xa_wo, ffn_w_up, ffn_conv, ffn_w_down, loss_target, m_norm_mix_g, m_norm_xa_g, m_norm_ffn_g, m_norm_mem_g, m_norm_final_g, m_w_in_ab, m_conv_qkv_a, m_a_log_a, m_dt_bias_a, m_onorm_g_a, m_ssm_lambda_re, m_ssm_lambda_im, m_ssm_b_re, m_ssm_b_im, m_ssm_c_re, m_ssm_c_im, m_ssm_d, m_ssm_log_dt, m_w_glu_b, m_b_glu_b, m_w_out_ab, m_pool_w, m_pool_scale, m_xa_wq, m_xa_wkv, m_xa_wo, m_ffn_w_up, m_ffn_conv, m_ffn_w_down, v_norm_mix_g, v_norm_xa_g, v_norm_ffn_g, v_norm_mem_g, v_norm_final_g, v_w_in_ab, v_conv_qkv_a, v_a_log_a, v_dt_bias_a, v_onorm_g_a, v_ssm_lambda_re, v_ssm_lambda_im, v_ssm_b_re, v_ssm_b_im, v_ssm_c_re, v_ssm_c_im, v_ssm_d, v_ssm_log_dt, v_w_glu_b, v_b_glu_b, v_w_out_ab, v_pool_w, v_pool_scale, v_xa_wq, v_xa_wkv, v_xa_wo, v_ffn_w_up, v_ffn_conv, v_ffn_w_down)
    return _step(dict(zip(INPUT_NAMES, args)))
```

```python
import functools
import math

import numpy as np
import jax
import jax.numpy as jnp
from jax import lax
from jax.experimental import pallas as pl
from jax.experimental.pallas import tpu as pltpu
from jax.experimental.pallas import tpu_sc as plsc

F32, BF16 = jnp.float32, jnp.bfloat16
HIGH, HIGHEST = lax.Precision.HIGH, lax.Precision.HIGHEST
MESH = pl.DeviceIdType.MESH

N_DEV = 8
SEQ, D_MODEL, MEM_LEN = 2048, 1024, 256
WIDTH_A, N_HEADS_A, HEAD_A, CONV_A = 512, 4, 128, 4
GDR_CHUNK = 128
GDR_HEADS_PER_STEP = 4
SSM_WIDTH, SSM_GROUP, N_GROUPS, SSM_STATE = 512, 16, 32, 64
SSM_CH = N_GROUPS * SSM_STATE
SCAN_CB = 512
POOL_WINDOWS = (2, 4, 8, 16)
POOL_GROUP = 256
N_HEADS_X, HEAD_X = 4, 256
D_FF, CONV_FFN = 2816, 3
RMS_EPS = 1e-6
ADAM_LR, ADAM_B1, ADAM_B2, ADAM_EPS, ADAM_WD, ADAM_STEP = 0.001, 0.9, 0.999, 1e-08, 0.01, 10
LANE = 128
PACK_COLS = 1024
VMEM_LIMIT_BYTES = 56 * 1024 * 1024


def _params(sem=None):
    return pltpu.CompilerParams(dimension_semantics=sem, vmem_limit_bytes=VMEM_LIMIT_BYTES)


class Comm:
    def __init__(self, inputs, out_shapes, sems, start, end, mid=None):
        self.inputs, self.out_shapes, self.sems = list(inputs), list(out_shapes), list(sems)
        self.start, self.mid, self.end = start, mid, end


def _merge_comms(comms):
    comms = [c for c in comms if c is not None]
    if not comms:
        return None, []
    bounds, ni, no, ns = [], 0, 0, 0
    for c in comms:
        bounds.append((ni, no, ns))
        ni, no, ns = ni + len(c.inputs), no + len(c.out_shapes), ns + len(c.sems)

    def phase(which):
        def run(ins, outs, sems):
            for c, (i0, o0, s0) in zip(comms, bounds):
                fn = getattr(c, which)
                if fn is not None:
                    fn(ins[i0:i0 + len(c.inputs)], outs[o0:o0 + len(c.out_shapes)], sems[s0:s0 + len(c.sems)])
        return run

    merged = Comm([a for c in comms for a in c.inputs], [s for c in comms for s in c.out_shapes],
                  [s for c in comms for s in c.sems], phase("start"), phase("end"), phase("mid"))
    return merged, [(o0, o0 + len(c.out_shapes)) for c, (_, o0, _) in zip(comms, bounds)]


def _call(body, *, name, grid, in_specs, out_specs, out_shape, args, scratch_shapes=(), sem=None, comm=None):
    single = not isinstance(out_shape, (list, tuple))
    out_specs_l = [out_specs] if single else list(out_specs)
    out_shape_l = [out_shape] if single else list(out_shape)
    scratch_shapes = list(scratch_shapes)
    merged, spans = _merge_comms(comm if isinstance(comm, (list, tuple)) else [comm])
    if merged is None:
        outs = pl.pallas_call(body, name=name, grid=grid, in_specs=list(in_specs), out_specs=out_specs_l,
                              out_shape=out_shape_l, scratch_shapes=scratch_shapes, compiler_params=_params(sem))(*args)
        outs = outs[0] if single else outs
        return outs if comm is None else (outs, [])
    n_in, n_out, n_scr = len(in_specs), len(out_specs_l), len(scratch_shapes)
    ci, co = len(merged.inputs), len(merged.out_shapes)
    total = int(np.prod(grid))

    def wrapped(*refs):
        ins, cins = refs[:n_in], refs[n_in:n_in + ci]
        outs, couts = refs[n_in + ci:n_in + ci + n_out], refs[n_in + ci + n_out:n_in + ci + n_out + co]
        scr, csems = refs[n_in + ci + n_out + co:n_in + ci + n_out + co + n_scr], refs[n_in + ci + n_out + co + n_scr:]
        lin = pl.program_id(0)
        for d in range(1, len(grid)):
            lin = lin * grid[d] + pl.program_id(d)
        pl.when(lin == 0)(lambda: merged.start(cins, couts, csems))
        body(*ins, *outs, *scr)
        mid_step = min((3 * total) // 4, total - 1)
        pl.when(lin == mid_step)(lambda: merged.mid(cins, couts, csems))
        pl.when(lin == total - 1)(lambda: merged.end(cins, couts, csems))

    any_spec = pl.BlockSpec(memory_space=pl.ANY)
    res = pl.pallas_call(
        wrapped, name=name, grid=grid, in_specs=list(in_specs) + [any_spec] * ci,
        out_specs=out_specs_l + [any_spec] * co, out_shape=out_shape_l + merged.out_shapes,
        scratch_shapes=scratch_shapes + merged.sems,
        compiler_params=_params(("arbitrary",) * len(grid)))(*args, *merged.inputs)
    outs, couts = res[:n_out], res[n_out:]
    return (outs[0] if single else list(outs)), [list(couts[a:b]) for a, b in spans]


def _comm_only(comm, name):
    def body():
        pass

    _, couts = _call(body, name=name, grid=(1,), in_specs=[], out_specs=[], out_shape=[], args=[], comm=comm)
    return couts[0]


def _tile(dim, pref):
    best = None
    for t in range(LANE, min(dim, pref) + 1, LANE):
        if dim % t == 0:
            best = t
    return best if best is not None else dim


MM_VMEM_BUDGET = 40 * 1024 * 1024


def _mm_tiles(m, n, k, a_bytes, b_bytes, o_bytes, r_bytes):
    for tk in (k, _tile(k, 2048), _tile(k, 1024), _tile(k, 512)):
        for tm, tn in ((1024, 1536), (1024, 1024), (1024, 512), (512, 512), (256, 512), (256, 256)):
            tm, tn = _tile(m, tm), _tile(n, tn)
            acc = 0 if tk == k else tm * tn * 4
            need = 2 * (tm * tk * a_bytes + tk * tn * b_bytes + tm * tn * (o_bytes + r_bytes)) + acc
            if need <= MM_VMEM_BUDGET:
                return tm, tn, tk
    raise ValueError("no matmul tiling fits VMEM")


def _mm(a, b, mode, name, out_dtype=F32, res=None, comm=None):
    if mode == "nn":
        (m, k), n = a.shape, b.shape[1]
    elif mode == "nt":
        (m, k), n = a.shape, b.shape[0]
    else:
        (k, m), n = a.shape, b.shape[1]
    tm, tn, tk = _mm_tiles(m, n, k, a.dtype.itemsize, b.dtype.itemsize, jnp.dtype(out_dtype).itemsize,
                           0 if res is None else res.dtype.itemsize)
    nk = k // tk
    dims = {"nn": ((1,), (0,)), "nt": ((1,), (1,)), "tn": ((0,), (0,))}[mode]

    def body(*refs):
        if res is None:
            a_ref, b_ref, o_ref = refs[:3]
            r_ref = None
        else:
            a_ref, b_ref, r_ref, o_ref = refs[:4]
        part = lax.dot_general(a_ref[...].astype(BF16), b_ref[...].astype(BF16), (dims, ((), ())),
                               preferred_element_type=F32)

        def finish(out):
            if r_ref is not None:
                out = out + r_ref[...].astype(F32)
            o_ref[...] = out.astype(out_dtype)

        if nk == 1:
            finish(part)
            return
        acc = refs[-1]
        kk = pl.program_id(2)

        @pl.when(kk == 0)
        def _():
            acc[...] = part

        @pl.when(kk > 0)
        def _():
            acc[...] += part

        @pl.when(kk == nk - 1)
        def _():
            finish(acc[...])

    a_spec = (pl.BlockSpec((tk, tm), lambda i, j, q: (q, i)) if mode == "tn"
              else pl.BlockSpec((tm, tk), lambda i, j, q: (i, q)))
    b_spec = (pl.BlockSpec((tn, tk), lambda i, j, q: (j, q)) if mode == "nt"
              else pl.BlockSpec((tk, tn), lambda i, j, q: (q, j)))
    o_spec = pl.BlockSpec((tm, tn), lambda i, j, q: (i, j))
    in_specs, args = [a_spec, b_spec], [a, b]
    if res is not None:
        in_specs.append(o_spec)
        args.append(res)
    return _call(body, name=name, grid=(m // tm, n // tn, nk), in_specs=in_specs, out_specs=o_spec,
                 out_shape=jax.ShapeDtypeStruct((m, n), out_dtype),
                 scratch_shapes=[] if nk == 1 else [pltpu.VMEM((tm, tn), F32)],
                 sem=("parallel", "parallel", "arbitrary"), args=args, comm=comm)


def _mm_bd(a, b, mode, name, out_dtype=F32, res=None, comm=None, tm=1024):
    if mode == "tn":
        k = a.shape[0]
        nb = min(a.shape[1], b.shape[1]) // LANE
        ma, n = a.shape[1] // nb, b.shape[1] // nb

        def body(a_ref, b_ref, o_ref):
            o_ref[0] = lax.dot_general(a_ref[...].astype(BF16), b_ref[...].astype(BF16), (((0,), (0,)), ((), ())),
                                       preferred_element_type=F32).astype(out_dtype)

        return _call(body, name=name, grid=(nb,),
                     in_specs=[pl.BlockSpec((k, ma), lambda j: (0, j)), pl.BlockSpec((k, n), lambda j: (0, j))],
                     out_specs=pl.BlockSpec((1, ma, n), lambda j: (j, 0, 0)),
                     out_shape=jax.ShapeDtypeStruct((nb, ma, n), out_dtype), sem=("parallel",), args=(a, b), comm=comm)
    m = a.shape[0]
    nb = b.shape[0]
    ka = a.shape[1] // nb
    n = b.shape[2] if mode == "nn" else b.shape[1]
    tm = _tile(m, tm)
    dims = ((1,), (0,)) if mode == "nn" else ((1,), (1,))

    def body(*refs):
        if res is None:
            a_ref, b_ref, o_ref = refs
            r_ref = None
        else:
            a_ref, b_ref, r_ref, o_ref = refs
        out = lax.dot_general(a_ref[...].astype(BF16), b_ref[0].astype(BF16), (dims, ((), ())),
                              preferred_element_type=F32)
        if r_ref is not None:
            out = out + r_ref[...].astype(F32)
        o_ref[...] = out.astype(out_dtype)

    o_spec = pl.BlockSpec((tm, n), lambda i, j: (i, j))
    in_specs = [pl.BlockSpec((tm, ka), lambda i, j: (i, j)), pl.BlockSpec((1,) + b.shape[1:], lambda i, j: (j, 0, 0))]
    args = [a, b]
    if res is not None:
        in_specs.append(o_spec)
        args.append(res)
    return _call(body, name=name, grid=(m // tm, nb), in_specs=in_specs, out_specs=o_spec,
                 out_shape=jax.ShapeDtypeStruct((m, nb * n), out_dtype), sem=("parallel", "parallel"),
                 args=args, comm=comm)


def _rms_fwd(x, g, out_dtype, name, tr=256):
    rows, d = x.shape

    def body(x_ref, g_ref, o_ref):
        xv = x_ref[...]
        r = lax.rsqrt(jnp.mean(xv * xv, axis=-1, keepdims=True) + RMS_EPS)
        o_ref[...] = (xv * r * g_ref[...]).astype(out_dtype)

    return pl.pallas_call(
        body, name=name, grid=(rows // tr,),
        in_specs=[pl.BlockSpec((tr, d), lambda i: (i, 0)), pl.BlockSpec((1, d), lambda i: (0, 0))],
        out_specs=pl.BlockSpec((tr, d), lambda i: (i, 0)), out_shape=jax.ShapeDtypeStruct((rows, d), out_dtype),
        compiler_params=_params(("parallel",)))(x, g.reshape(1, d))


def _rms_bwd(x, g, dy, dres, name, tr=256, comm=None):
    rows, d = x.shape

    def body(*refs):
        if dres is None:
            x_ref, g_ref, dy_ref, dx_ref, dg_ref = refs
            r_ref = None
        else:
            x_ref, g_ref, dy_ref, r_ref, dx_ref, dg_ref = refs

        @pl.when(pl.program_id(0) == 0)
        def _():
            dg_ref[...] = jnp.zeros_like(dg_ref)

        xv, dyv = x_ref[...], dy_ref[...].astype(F32)
        r = lax.rsqrt(jnp.mean(xv * xv, axis=-1, keepdims=True) + RMS_EPS)
        xh = xv * r
        dyg = dyv * g_ref[...]
        dx = r * (dyg - xh * jnp.mean(dyg * xh, axis=-1, keepdims=True))
        if r_ref is not None:
            dx = dx + r_ref[...]
        dx_ref[...] = dx
        dg_ref[...] += jnp.sum(dyv * xh, axis=0, keepdims=True)

    blk = pl.BlockSpec((tr, d), lambda i: (i, 0))
    vec = pl.BlockSpec((1, d), lambda i: (0, 0))
    in_specs, args = [blk, vec, blk], [x, g.reshape(1, d), dy]
    if dres is not None:
        in_specs.append(blk)
        args.append(dres)
    return _call(
        body, name=name, grid=(rows // tr,), in_specs=in_specs, out_specs=[blk, vec],
        out_shape=[jax.ShapeDtypeStruct((rows, d), F32), jax.ShapeDtypeStruct((1, d), F32)],
        sem=("arbitrary",), args=args, comm=comm)


def _loss_head(x, g, target, name, tr=256):
    rows, d = x.shape

    def body(x_ref, g_ref, t_ref, loss_ref, dx_ref, dg_ref):
        @pl.when(pl.program_id(0) == 0)
        def _():
            dg_ref[...] = jnp.zeros_like(dg_ref)
            loss_ref[...] = jnp.zeros_like(loss_ref)

        xv = x_ref[...]
        r = lax.rsqrt(jnp.mean(xv * xv, axis=-1, keepdims=True) + RMS_EPS)
        xh = xv * r
        err = xh * g_ref[...] - t_ref[...]
        loss_ref[...] += 0.5 * jnp.sum(jnp.mean(err * err, axis=-1, keepdims=True), keepdims=True)
        dyv = err * (1.0 / d)
        dyg = dyv * g_ref[...]
        dx_ref[...] = r * (dyg - xh * jnp.mean(dyg * xh, axis=-1, keepdims=True))
        dg_ref[...] += jnp.sum(dyv * xh, axis=0, keepdims=True)

    blk = pl.BlockSpec((tr, d), lambda i: (i, 0))
    vec = pl.BlockSpec((1, d), lambda i: (0, 0))
    return pl.pallas_call(
        body, name=name, grid=(rows // tr,), in_specs=[blk, vec, blk],
        out_specs=[pl.BlockSpec((1, 1), lambda i: (0, 0)), blk, vec],
        out_shape=[jax.ShapeDtypeStruct((1, 1), F32), jax.ShapeDtypeStruct((rows, d), F32),
                   jax.ShapeDtypeStruct((1, d), F32)],
        compiler_params=_params(("arbitrary",)))(x, g.reshape(1, d), target)


def _shift_down(x, s):
    rows = lax.broadcasted_iota(jnp.int32, x.shape, 0)
    return jnp.where(rows >= s, pltpu.roll(x, s, 0), 0.0)


def _shift_up(x, s):
    n = x.shape[0]
    rows = lax.broadcasted_iota(jnp.int32, x.shape, 0)
    return jnp.where(rows < n - s, pltpu.roll(x, n - s, 0), 0.0)


def _sigmoid(x):
    return 1.0 / (1.0 + jnp.exp(-x))


def _silu_and_grad(x):
    s = _sigmoid(x)
    return x * s, s * (1.0 + x * (1.0 - s))


_GELU_C0, _GELU_C1 = math.sqrt(2.0 / math.pi), 0.044715


def _gelu_and_grad(x):
    th = jnp.tanh(_GELU_C0 * (x + _GELU_C1 * x * x * x))
    y = 0.5 * x * (1.0 + th)
    dy = 0.5 * (1.0 + th) + 0.5 * x * (1.0 - th * th) * _GELU_C0 * (1.0 + 3.0 * _GELU_C1 * x * x)
    return y, dy


def _ffn_act_fwd(h, w, name, tc=256, comm=None):
    t = h.shape[0]
    nb = D_FF // tc

    def body(hg_ref, hv_ref, wg_ref, wv_ref, a_ref):
        def conv(x, wr):
            return wr[2:3, :] * x + wr[1:2, :] * _shift_down(x, 1) + wr[0:1, :] * _shift_down(x, 2)

        cg = conv(hg_ref[...], wg_ref[...])
        cv = conv(hv_ref[...], wv_ref[...])
        a_ref[...] = (cg * _sigmoid(cg) * cv).astype(BF16)

    return _call(
        body, name=name, grid=(nb,),
        in_specs=[pl.BlockSpec((t, tc), lambda j: (0, j)), pl.BlockSpec((t, tc), lambda j: (0, j + nb)),
                  pl.BlockSpec((CONV_FFN, tc), lambda j: (0, j)), pl.BlockSpec((CONV_FFN, tc), lambda j: (0, j + nb))],
        out_specs=pl.BlockSpec((t, tc), lambda j: (0, j)), out_shape=jax.ShapeDtypeStruct((t, D_FF), BF16),
        sem=("parallel",), args=(h, h, w, w), comm=comm)


def _ffn_act_bwd(h, w, da, name, tc=256, comm=None):
    t = h.shape[0]
    nb = D_FF // tc

    def body(hg_ref, hv_ref, wg_ref, wv_ref, da_ref, dhg_ref, dhv_ref, dwg_ref, dwv_ref):
        hg, hv, wg, wv = hg_ref[...], hv_ref[...], wg_ref[...], wv_ref[...]
        hg1, hg2, hv1, hv2 = _shift_down(hg, 1), _shift_down(hg, 2), _shift_down(hv, 1), _shift_down(hv, 2)
        cg = wg[2:3, :] * hg + wg[1:2, :] * hg1 + wg[0:1, :] * hg2
        cv = wv[2:3, :] * hv + wv[1:2, :] * hv1 + wv[0:1, :] * hv2
        sg, dsg = _silu_and_grad(cg)
        dav = da_ref[...].astype(F32)
        dcv = dav * sg
        dcg = dav * cv * dsg

        def conv_t(dc, wr):
            return wr[2:3, :] * dc + wr[1:2, :] * _shift_up(dc, 1) + wr[0:1, :] * _shift_up(dc, 2)

        dhg_ref[...] = conv_t(dcg, wg).astype(BF16)
        dhv_ref[...] = conv_t(dcv, wv).astype(BF16)
        dwg_ref[0:1, :] = jnp.sum(dcg * hg2, axis=0, keepdims=True)
        dwg_ref[1:2, :] = jnp.sum(dcg * hg1, axis=0, keepdims=True)
        dwg_ref[2:3, :] = jnp.sum(dcg * hg, axis=0, keepdims=True)
        dwv_ref[0:1, :] = jnp.sum(dcv * hv2, axis=0, keepdims=True)
        dwv_ref[1:2, :] = jnp.sum(dcv * hv1, axis=0, keepdims=True)
        dwv_ref[2:3, :] = jnp.sum(dcv * hv, axis=0, keepdims=True)

    big = lambda off: pl.BlockSpec((t, tc), lambda j: (0, j + off))
    small = lambda off: pl.BlockSpec((CONV_FFN, tc), lambda j: (0, j + off))
    res = _call(
        body, name=name, grid=(nb,),
        in_specs=[big(0), big(nb), small(0), small(nb), big(0)],
        out_specs=[big(0), big(0), small(0), small(0)],
        out_shape=[jax.ShapeDtypeStruct((t, D_FF), BF16), jax.ShapeDtypeStruct((t, D_FF), BF16),
                   jax.ShapeDtypeStruct((CONV_FFN, D_FF), F32), jax.ShapeDtypeStruct((CONV_FFN, D_FF), F32)],
        sem=("parallel",), args=(h, h, w, w, da), comm=comm)
    (dhg, dhv, dwg, dwv), couts = res if comm is not None else (res, None)
    out = (jnp.concatenate([dhg, dhv], axis=1), jnp.concatenate([dwg, dwv], axis=1))
    return out if comm is None else (out, couts)


def _attn_probs(q, k):
    s = lax.dot_general(q.astype(BF16), k.astype(BF16), (((1,), (1,)), ((), ())),
                        preferred_element_type=F32) * (HEAD_X ** -0.5)
    s = s - jnp.max(s, axis=-1, keepdims=True)
    p = jnp.exp(s)
    return p / jnp.sum(p, axis=-1, keepdims=True)


def _attn_fwd(q, kv, name, tq=512, comm=None):
    t = q.shape[0]

    def body(q_ref, k_ref, v_ref, o_ref):
        p = _attn_probs(q_ref[...], k_ref[...])
        o_ref[...] = jnp.dot(p.astype(BF16), v_ref[...].astype(BF16), preferred_element_type=F32).astype(BF16)

    return _call(
        body, name=name, grid=(N_HEADS_X, t // tq),
        in_specs=[pl.BlockSpec((tq, HEAD_X), lambda h, i: (i, h)),
                  pl.BlockSpec((MEM_LEN, HEAD_X), lambda h, i: (0, h)),
                  pl.BlockSpec((MEM_LEN, HEAD_X), lambda h, i: (0, h + N_HEADS_X))],
        out_specs=pl.BlockSpec((tq, HEAD_X), lambda h, i: (i, h)),
        out_shape=jax.ShapeDtypeStruct((t, N_HEADS_X * HEAD_X), BF16),
        sem=("parallel", "parallel"), args=(q, kv, kv), comm=comm)


def _attn_bwd(q, kv, do, name, tq=512):
    t = q.shape[0]

    def body(q_ref, k_ref, v_ref, do_ref, dq_ref, dk_ref, dv_ref):
        @pl.when(pl.program_id(1) == 0)
        def _():
            dk_ref[...] = jnp.zeros_like(dk_ref)
            dv_ref[...] = jnp.zeros_like(dv_ref)

        qb, kb, vb, dob = (r[...].astype(BF16) for r in (q_ref, k_ref, v_ref, do_ref))
        p = _attn_probs(qb, kb)
        dp = lax.dot_general(dob, vb, (((1,), (1,)), ((), ())), preferred_element_type=F32)
        ds = p * (dp - jnp.sum(dp * p, axis=-1, keepdims=True)) * (HEAD_X ** -0.5)
        dsb = ds.astype(BF16)
        dq_ref[...] = jnp.dot(dsb, kb, preferred_element_type=F32).astype(BF16)
        dk_ref[...] += lax.dot_general(dsb, qb, (((0,), (0,)), ((), ())), preferred_element_type=F32)
        dv_ref[...] += lax.dot_general(p.astype(BF16), dob, (((0,), (0,)), ((), ())), preferred_element_type=F32)

    qs = pl.BlockSpec((tq, HEAD_X), lambda h, i: (i, h))
    ms = pl.BlockSpec((MEM_LEN, HEAD_X), lambda h, i: (0, h))
    return pl.pallas_call(
        body, name=name, grid=(N_HEADS_X, t // tq),
        in_specs=[qs, ms, pl.BlockSpec((MEM_LEN, HEAD_X), lambda h, i: (0, h + N_HEADS_X)), qs],
        out_specs=[qs, ms, ms],
        out_shape=[jax.ShapeDtypeStruct((t, D_MODEL), BF16), jax.ShapeDtypeStruct((MEM_LEN, D_MODEL), F32),
                   jax.ShapeDtypeStruct((MEM_LEN, D_MODEL), F32)],
        compiler_params=_params(("parallel", "arbitrary")))(q, kv, kv, do)


def _pool_counts(t, win):
    pos = lax.broadcasted_iota(jnp.int32, (t, 1), 0).astype(F32) + 1.0
    return 1.0 / jnp.minimum(pos, float(win))


def _pool_delta(xv, win):
    s, step = xv, 1
    while step < win:
        s = s + _shift_down(s, step)
        step *= 2
    return s * _pool_counts(xv.shape[0], win) - xv


def _pool_delta_t(dv, win):
    s, step = dv * _pool_counts(dv.shape[0], win), 1
    while step < win:
        s = s + _shift_up(s, step)
        step *= 2
    return s - dv


def _pool_fwd(xn, w, scale, res, name):
    t = xn.shape[0]

    def make_branch(win, xn_ref, w_ref, s_ref, r_ref, o_ref):
        def branch():
            dl = _pool_delta(xn_ref[...], win)
            y = jnp.dot(dl.astype(BF16), w_ref[0], preferred_element_type=F32)
            o_ref[...] = r_ref[...] + y * s_ref[...]
        return branch

    def body(xn_ref, w_ref, s_ref, r_ref, o_ref):
        for gi, win in enumerate(POOL_WINDOWS):
            pl.when(pl.program_id(0) == gi)(make_branch(win, xn_ref, w_ref, s_ref, r_ref, o_ref))

    blk = pl.BlockSpec((t, POOL_GROUP), lambda g: (0, g))
    return pl.pallas_call(
        body, name=name, grid=(len(POOL_WINDOWS),),
        in_specs=[blk, pl.BlockSpec((1, POOL_GROUP, POOL_GROUP), lambda g: (g, 0, 0)),
                  pl.BlockSpec((1, POOL_GROUP), lambda g: (0, g)), blk],
        out_specs=blk, out_shape=jax.ShapeDtypeStruct((t, D_MODEL), F32),
        compiler_params=_params(("parallel",)))(xn, w, scale, res)


def _pool_bwd(xn, w, scale, dmix, name):
    t = xn.shape[0]

    def make_branch(win, xn_ref, w_ref, s_ref, d_ref, dxn_ref, dw_ref, ds_ref):
        def branch():
            dl = _pool_delta(xn_ref[...], win).astype(BF16)
            wv = w_ref[0]
            dm = d_ref[...]
            y = jnp.dot(dl, wv, preferred_element_type=F32)
            ds_ref[...] = jnp.sum(dm * y, axis=0, keepdims=True)
            dy = (dm * s_ref[...]).astype(BF16)
            dw_ref[0] = lax.dot_general(dl, dy, (((0,), (0,)), ((), ())), preferred_element_type=F32)
            ddl = lax.dot_general(dy, wv, (((1,), (1,)), ((), ())), preferred_element_type=F32)
            dxn_ref[...] = _pool_delta_t(ddl, win)
        return branch

    def body(*refs):
        for gi, win in enumerate(POOL_WINDOWS):
            pl.when(pl.program_id(0) == gi)(make_branch(win, *refs))

    blk = pl.BlockSpec((t, POOL_GROUP), lambda g: (0, g))
    wspec = pl.BlockSpec((1, POOL_GROUP, POOL_GROUP), lambda g: (g, 0, 0))
    vec = pl.BlockSpec((1, POOL_GROUP), lambda g: (0, g))
    return pl.pallas_call(
        body, name=name, grid=(len(POOL_WINDOWS),), in_specs=[blk, wspec, vec, blk], out_specs=[blk, wspec, vec],
        out_shape=[jax.ShapeDtypeStruct((t, D_MODEL), F32),
                   jax.ShapeDtypeStruct((len(POOL_WINDOWS), POOL_GROUP, POOL_GROUP), F32),
                   jax.ShapeDtypeStruct((1, D_MODEL), F32)],
        compiler_params=_params(("parallel",)))(xn, w, scale, dmix)


def _qkv_conv(h, wr):
    return (wr[3:4, :] * h + wr[2:3, :] * _shift_down(h, 1) + wr[1:2, :] * _shift_down(h, 2)
            + wr[0:1, :] * _shift_down(h, 3))


def _qkv_pre_fwd(h, w, col0, ncols, normalize, scale, name):
    t = h.shape[0]

    def body(h_ref, w_ref, o_ref):
        c = _qkv_conv(h_ref[...], w_ref[...])
        s = c * _sigmoid(c)
        if normalize:
            s = s * lax.rsqrt(jnp.sum(s * s, axis=-1, keepdims=True) + 1e-6) * scale
        o_ref[...] = s

    return pl.pallas_call(
        body, name=name, grid=(ncols,),
        in_specs=[pl.BlockSpec((t, HEAD_A), lambda j: (0, j + col0)), pl.BlockSpec((CONV_A, HEAD_A), lambda j: (0, j + col0))],
        out_specs=pl.BlockSpec((t, HEAD_A), lambda j: (0, j)), out_shape=jax.ShapeDtypeStruct((t, ncols * HEAD_A), F32),
        compiler_params=_params(("parallel",)))(h, w)


def _qkv_pre_bwd(h, w, dy, col0, ncols, normalize, scale, name):
    t = h.shape[0]

    def body(h_ref, w_ref, dy_ref, dh_ref, dw_ref):
        hv, wr, dyv = h_ref[...], w_ref[...], dy_ref[...]
        h1, h2, h3 = _shift_down(hv, 1), _shift_down(hv, 2), _shift_down(hv, 3)
        c = wr[3:4, :] * hv + wr[2:3, :] * h1 + wr[1:2, :] * h2 + wr[0:1, :] * h3
        s, dsilu = _silu_and_grad(c)
        if normalize:
            r = lax.rsqrt(jnp.sum(s * s, axis=-1, keepdims=True) + 1e-6)
            y = s * r
            dyv = dyv * scale
            ds = r * (dyv - y * jnp.sum(dyv * y, axis=-1, keepdims=True))
        else:
            ds = dyv
        dc = ds * dsilu
        dh = (wr[3:4, :] * dc + wr[2:3, :] * _shift_up(dc, 1) + wr[1:2, :] * _shift_up(dc, 2)
              + wr[0:1, :] * _shift_up(dc, 3))
        dh_ref[...] = dh.astype(BF16)
        dw_ref[0:1, :] = jnp.sum(dc * h3, axis=0, keepdims=True)
        dw_ref[1:2, :] = jnp.sum(dc * h2, axis=0, keepdims=True)
        dw_ref[2:3, :] = jnp.sum(dc * h1, axis=0, keepdims=True)
        dw_ref[3:4, :] = jnp.sum(dc * hv, axis=0, keepdims=True)

    return pl.pallas_call(
        body, name=name, grid=(ncols,),
        in_specs=[pl.BlockSpec((t, HEAD_A), lambda j: (0, j + col0)), pl.BlockSpec((CONV_A, HEAD_A), lambda j: (0, j + col0)),
                  pl.BlockSpec((t, HEAD_A), lambda j: (0, j))],
        out_specs=[pl.BlockSpec((t, HEAD_A), lambda j: (0, j)), pl.BlockSpec((CONV_A, HEAD_A), lambda j: (0, j))],
        out_shape=[jax.ShapeDtypeStruct((t, ncols * HEAD_A), BF16), jax.ShapeDtypeStruct((CONV_A, ncols * HEAD_A), F32)],
        compiler_params=_params(("parallel",)))(h, w, dy)


def _softplus(x):
    return jnp.maximum(x, 0.0) + jnp.log1p(jnp.exp(-jnp.abs(x)))


def _gates_fwd(ba, arow, brow, name):
    t = ba.shape[0]

    def body(x_ref, a_ref, b_ref, o_ref):
        xv = x_ref[...]
        lane = lax.broadcasted_iota(jnp.int32, xv.shape, 1)
        beta = _sigmoid(xv)
        g = -jnp.exp(a_ref[...]) * _softplus(xv + b_ref[...])
        o_ref[...] = jnp.where(lane < N_HEADS_A, beta, jnp.where(lane < 2 * N_HEADS_A, g, 0.0))

    return pl.pallas_call(body, name=name, out_shape=jax.ShapeDtypeStruct((t, LANE), F32),
                          compiler_params=_params())(ba, arow, brow)


def _gates_bwd(ba, arow, brow, dgb, name):
    t = ba.shape[0]

    def body(x_ref, a_ref, b_ref, d_ref, dx_ref, da_ref, db_ref):
        xv = x_ref[...]
        dv = d_ref[0] + d_ref[1] + d_ref[2] + d_ref[3]
        lane = lax.broadcasted_iota(jnp.int32, xv.shape, 1)
        beta = _sigmoid(xv)
        ea = jnp.exp(a_ref[...])
        z = xv + b_ref[...]
        dgv = jnp.where((lane >= N_HEADS_A) & (lane < 2 * N_HEADS_A), dv, 0.0) * (-ea)
        dz = dgv * _sigmoid(z)
        dx = jnp.where(lane < N_HEADS_A, dv * beta * (1.0 - beta), dz)
        dx_ref[...] = dx.astype(BF16)
        db_ref[...] = jnp.sum(dz, axis=0, keepdims=True)
        da_ref[...] = jnp.sum(dgv * _softplus(z), axis=0, keepdims=True)

    return pl.pallas_call(
        body, name=name,
        out_shape=[jax.ShapeDtypeStruct((t, LANE), BF16), jax.ShapeDtypeStruct((1, LANE), F32),
                   jax.ShapeDtypeStruct((1, LANE), F32)],
        compiler_params=_params())(ba, arow, brow, dgb)


def _dot(a, b, prec=None):
    if prec is None:
        return jnp.dot(a.astype(BF16), b.astype(BF16), preferred_element_type=F32)
    return jnp.dot(a, b, precision=prec, preferred_element_type=F32)


def _dot_nt(a, b, prec=None):
    if prec is None:
        a, b = a.astype(BF16), b.astype(BF16)
    return lax.dot_general(a, b, (((1,), (1,)), ((), ())), precision=prec, preferred_element_type=F32)


def _dot_tn(a, b, prec=None):
    if prec is None:
        a, b = a.astype(BF16), b.astype(BF16)
    return lax.dot_general(a, b, (((0,), (0,)), ((), ())), precision=prec, preferred_element_type=F32)


def _gdr_chunk_terms(k, beta, g):
    c = GDR_CHUNK
    row = lax.broadcasted_iota(jnp.int32, (c, c), 0)
    col = lax.broadcasted_iota(jnp.int32, (c, c), 1)
    causal, strict = row >= col, row > col
    gcum = _dot(causal.astype(F32), jnp.broadcast_to(g, (c, c)), HIGHEST)
    diff = gcum - gcum.T
    decay = jnp.where(causal, jnp.exp(jnp.where(causal, diff, 0.0)), 0.0)
    kb = k * beta
    kk = _dot_nt(kb, k)
    return row, col, causal, strict, gcum, decay, kb, kk


def _unit_lower_inverse(a):
    c = a.shape[0]
    eye = (lax.broadcasted_iota(jnp.int32, (c, c), 0) == lax.broadcasted_iota(jnp.int32, (c, c), 1)).astype(F32)
    p = -a
    inv = eye + p
    step = 1
    while 2 * step < c:
        p = _dot(p, p, HIGH)
        inv = inv + _dot(inv, p, HIGH)
        step *= 2
    return inv


def _head_gates(gates, head):
    lane = lax.broadcasted_iota(jnp.int32, gates.shape, 1)
    beta = jnp.sum(jnp.where(lane == head, gates, 0.0), axis=1, keepdims=True)
    g = jnp.sum(jnp.where(lane == head + N_HEADS_A, gates, 0.0), axis=1, keepdims=True)
    return beta, g


def _gdr_fwd(q, k, v, gates, name, comm=None):
    t = q.shape[0]
    c = GDR_CHUNK
    n = t // c

    hps = GDR_HEADS_PER_STEP

    def one_head(hh, q_ref, k_ref, v_ref, gb_ref, o_ref, tm_ref, s_ref, state):
        cols = slice(hh * HEAD_A, (hh + 1) * HEAD_A)
        qv, kv, vv = q_ref[:, cols], k_ref[:, cols], v_ref[:, cols]
        beta, g = _head_gates(gb_ref[...], pl.program_id(0) * hps + hh)
        row, col, causal, strict, gcum, decay, kb, kk = _gdr_chunk_terms(kv, beta, g)
        tm = _unit_lower_inverse(jnp.where(strict, kk * decay, 0.0))
        e = jnp.exp(gcum)
        u = _dot(tm, vv * beta, HIGH)
        w = _dot(tm, kb * e, HIGH)
        p = jnp.where(causal, _dot_nt(qv, kv) * decay, 0.0)
        s = state[hh]
        s_ref[hh, 0] = s
        tm_ref[hh, 0] = tm
        vn = u - _dot(w, s)
        o_ref[:, cols] = _dot(qv * e, s) + _dot(p, vn)
        glast = gcum[c - 1:c, :]
        state[hh] = s * jnp.exp(glast) + _dot_tn(kv * jnp.exp(glast - gcum), vn)

    def body(*refs):
        state = refs[-1]

        @pl.when(pl.program_id(1) == 0)
        def _():
            state[...] = jnp.zeros_like(state)

        for hh in range(hps):
            one_head(hh, *refs)

    blk = pl.BlockSpec((c, hps * HEAD_A), lambda h, i: (i, h))
    mat = pl.BlockSpec((hps, 1, c, c), lambda h, i: (h, i, 0, 0))
    return _call(
        body, name=name, grid=(N_HEADS_A // hps, n),
        in_specs=[blk, blk, blk, pl.BlockSpec((c, LANE), lambda h, i: (i, 0))],
        out_specs=[blk, mat, mat],
        out_shape=[jax.ShapeDtypeStruct((t, WIDTH_A), F32), jax.ShapeDtypeStruct((N_HEADS_A, n, c, c), F32),
                   jax.ShapeDtypeStruct((N_HEADS_A, n, HEAD_A, HEAD_A), F32)],
        scratch_shapes=[pltpu.VMEM((hps, HEAD_A, HEAD_A), F32)], sem=("parallel", "arbitrary"),
        args=(q, k, v, gates), comm=comm)


def _gdr_bwd(q, k, v, gates, tm_all, s_all, do, name, comm=None):
    t = q.shape[0]
    c = GDR_CHUNK
    n = t // c

    hps = GDR_HEADS_PER_STEP

    def one_head(hh, q_ref, k_ref, v_ref, gb_ref, tm_ref, s_ref, do_ref, dq_ref, dk_ref, dv_ref, dgb_ref, dstate):
        cols = slice(hh * HEAD_A, (hh + 1) * HEAD_A)
        qv, kv, vv, dov = q_ref[:, cols], k_ref[:, cols], v_ref[:, cols], do_ref[:, cols]
        head = pl.program_id(0) * hps + hh
        beta, g = _head_gates(gb_ref[...], head)
        tm, s, dsp = tm_ref[hh, 0], s_ref[hh, 0], dstate[hh]
        row, col, causal, strict, gcum, decay, kb, kk = _gdr_chunk_terms(kv, beta, g)
        e = jnp.exp(gcum)
        vb, kbe = vv * beta, kb * e
        u = _dot(tm, vb, HIGH)
        w = _dot(tm, kbe, HIGH)
        qk = _dot_nt(qv, kv)
        p = jnp.where(causal, qk * decay, 0.0)
        vn = u - _dot(w, s)
        glast = gcum[c - 1:c, :]
        el = jnp.exp(glast)
        f = jnp.exp(glast - gcum)
        kd = kv * f
        qe = qv * e

        dvn = _dot_tn(p, dov) + _dot(kd, dsp)
        dglast = el[:, 0:1] * jnp.sum(s * dsp, keepdims=True)
        dkd = _dot_nt(vn, dsp)
        dk = dkd * f
        df = jnp.sum(dkd * kv, axis=1, keepdims=True) * f[:, 0:1]
        dglast = dglast + jnp.sum(df, keepdims=True)
        dgc = -df
        dp = jnp.where(causal, _dot_nt(dov, vn), 0.0)
        dqe = _dot_nt(dov, s)
        dq = dqe * e
        de = jnp.sum(dqe * qv, axis=1, keepdims=True)
        dstate[hh] = dsp * el + _dot_tn(qe, dov) - _dot_tn(w, dvn)
        dw = -_dot_nt(dvn, s)
        dvb = _dot_tn(tm, dvn, HIGH)
        dkbe = _dot_tn(tm, dw, HIGH)
        da = -jnp.where(strict, _dot_nt(dvb, u) + _dot_nt(dkbe, w), 0.0)
        dkk = da * decay
        dqk = dp * decay
        dd = da * kk + dp * qk
        dq = dq + _dot(dqk, kv)
        dk = dk + _dot_tn(dqk, qv)
        dkb = _dot(dkk, kv) + dkbe * e
        dk = dk + _dot_tn(dkk, kb)
        de = de + jnp.sum(dkbe * kb, axis=1, keepdims=True)
        dk = dk + dkb * beta
        dbeta = jnp.sum(dkb * kv, axis=1, keepdims=True) + jnp.sum(dvb * vv, axis=1, keepdims=True)
        m = dd * decay
        dgc = dgc + jnp.sum(m, axis=1, keepdims=True) - jnp.sum(m.T, axis=1, keepdims=True)
        dgc = dgc + de * e[:, 0:1]
        dgc = dgc + jnp.where(row[:, 0:1] == c - 1, dglast, 0.0)
        dg = _dot((row <= col).astype(F32), jnp.broadcast_to(dgc, (c, c)), HIGHEST)
        dq_ref[:, cols] = dq
        dk_ref[:, cols] = dk
        dv_ref[:, cols] = dvb * beta
        lane = lax.broadcasted_iota(jnp.int32, (c, LANE), 1)
        dgb_ref[hh] = jnp.where(lane == head, dbeta, jnp.where(lane == head + N_HEADS_A, dg, 0.0))

    def body(*refs):
        dstate = refs[-1]

        @pl.when(pl.program_id(1) == 0)
        def _():
            dstate[...] = jnp.zeros_like(dstate)

        for hh in range(hps):
            one_head(hh, *refs)

    blk = pl.BlockSpec((c, hps * HEAD_A), lambda h, i: (n - 1 - i, h))
    mat = pl.BlockSpec((hps, 1, c, c), lambda h, i: (h, n - 1 - i, 0, 0))
    return _call(
        body, name=name, grid=(N_HEADS_A // hps, n),
        in_specs=[blk, blk, blk, pl.BlockSpec((c, LANE), lambda h, i: (n - 1 - i, 0)), mat, mat, blk],
        out_specs=[blk, blk, blk, pl.BlockSpec((hps, c, LANE), lambda h, i: (h, n - 1 - i, 0))],
        out_shape=[jax.ShapeDtypeStruct((t, WIDTH_A), F32)] * 3 + [jax.ShapeDtypeStruct((N_HEADS_A, t, LANE), F32)],
        scratch_shapes=[pltpu.VMEM((hps, HEAD_A, HEAD_A), F32)], sem=("parallel", "arbitrary"),
        args=(q, k, v, gates, tm_all, s_all, do), comm=comm)


_B_NN, _B_NT, _B_TN = ((2,), (1,)), ((2,), (2,)), ((1,), (1,))


def _bdot(a, b, dims=_B_NN, prec=None):
    if prec is None:
        a, b = a.astype(BF16), b.astype(BF16)
    return lax.dot_general(a, b, (dims, ((0,), (0,))), precision=prec, preferred_element_type=F32)


def _heads_of(ref):
    return jnp.stack([ref[:, h * HEAD_A:(h + 1) * HEAD_A] for h in range(N_HEADS_A)])


def _all_head_gates(gates):
    pairs = [_head_gates(gates, h) for h in range(N_HEADS_A)]
    return jnp.stack([b for b, _ in pairs]), jnp.stack([g for _, g in pairs])


def _gdr_terms(k, beta, g):
    h, c = k.shape[0], GDR_CHUNK
    row = lax.broadcasted_iota(jnp.int32, (c, c), 0)
    col = lax.broadcasted_iota(jnp.int32, (c, c), 1)
    causal, strict = row >= col, row > col
    lower = jnp.broadcast_to(causal.astype(F32), (h, c, c))
    gcum = _bdot(lower, jnp.broadcast_to(g, (h, c, c)), prec=HIGHEST)
    diff = gcum - jnp.swapaxes(gcum, 1, 2)
    decay = jnp.where(causal, jnp.exp(jnp.where(causal, diff, 0.0)), 0.0)
    kb = k * beta
    return row, col, causal, strict, gcum, decay, kb, _bdot(kb, k, _B_NT)


def _unit_lower_inverses(a):
    c = a.shape[1]
    eye = (lax.broadcasted_iota(jnp.int32, (c, c), 0) == lax.broadcasted_iota(jnp.int32, (c, c), 1)).astype(F32)
    p = -a
    inv = eye + p
    step = 1
    while 2 * step < c:
        p = _bdot(p, p, prec=HIGH)
        inv = inv + _bdot(inv, p, prec=HIGH)
        step *= 2
    return inv


def _gdr_fwd(q, k, v, gates, name, comm=None):
    t = q.shape[0]
    c, nh = GDR_CHUNK, N_HEADS_A
    n = t // c

    def body(q_ref, k_ref, v_ref, gb_ref, o_ref, tm_ref, s_ref, state):
        @pl.when(pl.program_id(0) == 0)
        def _():
            state[...] = jnp.zeros_like(state)

        qv, kv, vv = _heads_of(q_ref), _heads_of(k_ref), _heads_of(v_ref)
        beta, g = _all_head_gates(gb_ref[...])
        row, col, causal, strict, gcum, decay, kb, kk = _gdr_terms(kv, beta, g)
        tm = _unit_lower_inverses(jnp.where(strict, kk * decay, 0.0))
        e = jnp.exp(gcum)
        u = _bdot(tm, vv * beta, prec=HIGH)
        w = _bdot(tm, kb * e, prec=HIGH)
        p = jnp.where(causal, _bdot(qv, kv, _B_NT) * decay, 0.0)
        s = state[...]
        s_ref[:, 0] = s
        tm_ref[:, 0] = tm
        vn = u - _bdot(w, s)
        o = _bdot(qv * e, s) + _bdot(p, vn)
        for h in range(nh):
            o_ref[:, h * HEAD_A:(h + 1) * HEAD_A] = o[h]
        glast = gcum[:, c - 1:c, :]
        state[...] = s * jnp.exp(glast) + _bdot(kv * jnp.exp(glast - gcum), vn, _B_TN)

    blk = pl.BlockSpec((c, WIDTH_A), lambda i: (i, 0))
    mat = pl.BlockSpec((nh, 1, c, c), lambda i: (0, i, 0, 0))
    return _call(
        body, name=name, grid=(n,), in_specs=[blk, blk, blk, pl.BlockSpec((c, LANE), lambda i: (i, 0))],
        out_specs=[blk, mat, mat],
        out_shape=[jax.ShapeDtypeStruct((t, WIDTH_A), F32), jax.ShapeDtypeStruct((nh, n, c, c), F32),
                   jax.ShapeDtypeStruct((nh, n, HEAD_A, HEAD_A), F32)],
        scratch_shapes=[pltpu.VMEM((nh, HEAD_A, HEAD_A), F32)], sem=("arbitrary",),
        args=(q, k, v, gates), comm=comm)


def _gdr_bwd(q, k, v, gates, tm_all, s_all, do, name, comm=None):
    t = q.shape[0]
    c, nh = GDR_CHUNK, N_HEADS_A
    n = t // c

    def body(q_ref, k_ref, v_ref, gb_ref, tm_ref, s_ref, do_ref, dq_ref, dk_ref, dv_ref, dgb_ref, dstate):
        @pl.when(pl.program_id(0) == 0)
        def _():
            dstate[...] = jnp.zeros_like(dstate)

        qv, kv, vv, dov = _heads_of(q_ref), _heads_of(k_ref), _heads_of(v_ref), _heads_of(do_ref)
        beta, g = _all_head_gates(gb_ref[...])
        tm, s, dsp = tm_ref[:, 0], s_ref[:, 0], dstate[...]
        row, col, causal, strict, gcum, decay, kb, kk = _gdr_terms(kv, beta, g)
        rowsum = lambda x: jnp.sum(x, axis=2, keepdims=True)
        e = jnp.exp(gcum)
        vb, kbe = vv * beta, kb * e
        u = _bdot(tm, vb, prec=HIGH)
        w = _bdot(tm, kbe, prec=HIGH)
        qk = _bdot(qv, kv, _B_NT)
        p = jnp.where(causal, qk * decay, 0.0)
        vn = u - _bdot(w, s)
        glast = gcum[:, c - 1:c, :]
        el = jnp.exp(glast)
        f = jnp.exp(glast - gcum)
        kd = kv * f
        qe = qv * e

        dvn = _bdot(p, dov, _B_TN) + _bdot(kd, dsp)
        dglast = el[:, :, 0:1] * jnp.sum(s * dsp, axis=(1, 2), keepdims=True)
        dkd = _bdot(vn, dsp, _B_NT)
        dk = dkd * f
        df = rowsum(dkd * kv) * f[:, :, 0:1]
        dglast = dglast + jnp.sum(df, axis=1, keepdims=True)
        dgc = -df
        dp = jnp.where(causal, _bdot(dov, vn, _B_NT), 0.0)
        dqe = _bdot(dov, s, _B_NT)
        dq = dqe * e
        de = rowsum(dqe * qv)
        dstate[...] = dsp * el + _bdot(qe, dov, _B_TN) - _bdot(w, dvn, _B_TN)
        dw = -_bdot(dvn, s, _B_NT)
        dvb = _bdot(tm, dvn, _B_TN, prec=HIGH)
        dkbe = _bdot(tm, dw, _B_TN, prec=HIGH)
        da = -jnp.where(strict, _bdot(dvb, u, _B_NT) + _bdot(dkbe, w, _B_NT), 0.0)
        dkk = da * decay
        dqk = dp * decay
        dd = da * kk + dp * qk
        dq = dq + _bdot(dqk, kv)
        dk = dk + _bdot(dqk, qv, _B_TN)
        dkb = _bdot(dkk, kv) + dkbe * e
        dk = dk + _bdot(dkk, kb, _B_TN)
        de = de + rowsum(dkbe * kb)
        dk = dk + dkb * beta
        dbeta = rowsum(dkb * kv) + rowsum(dvb * vv)
        m = dd * decay
        dgc = dgc + rowsum(m) - rowsum(jnp.swapaxes(m, 1, 2))
        dgc = dgc + de * e[:, :, 0:1]
        dgc = dgc + jnp.where(row[:, 0:1] == c - 1, dglast, 0.0)
        upper = jnp.broadcast_to((row <= col).astype(F32), (nh, c, c))
        dg = _bdot(upper, jnp.broadcast_to(dgc, (nh, c, c)), prec=HIGHEST)
        dv = dvb * beta
        for h in range(nh):
            cols = slice(h * HEAD_A, (h + 1) * HEAD_A)
            dq_ref[:, cols] = dq[h]
            dk_ref[:, cols] = dk[h]
            dv_ref[:, cols] = dv[h]
        head = lax.broadcasted_iota(jnp.int32, (nh, c, LANE), 0)
        lane = lax.broadcasted_iota(jnp.int32, (nh, c, LANE), 2)
        dgb_ref[...] = jnp.where(lane == head, dbeta, jnp.where(lane == head + nh, dg, 0.0))

    blk = pl.BlockSpec((c, WIDTH_A), lambda i: (n - 1 - i, 0))
    mat = pl.BlockSpec((nh, 1, c, c), lambda i: (0, n - 1 - i, 0, 0))
    return _call(
        body, name=name, grid=(n,),
        in_specs=[blk, blk, blk, pl.BlockSpec((c, LANE), lambda i: (n - 1 - i, 0)), mat, mat, blk],
        out_specs=[blk, blk, blk, pl.BlockSpec((nh, c, LANE), lambda i: (0, n - 1 - i, 0))],
        out_shape=[jax.ShapeDtypeStruct((t, WIDTH_A), F32)] * 3 + [jax.ShapeDtypeStruct((nh, t, LANE), F32)],
        scratch_shapes=[pltpu.VMEM((nh, HEAD_A, HEAD_A), F32)], sem=("arbitrary",),
        args=(q, k, v, gates, tm_all, s_all, do), comm=comm)


def _onorm_fwd(o, gate, g, name):
    t = o.shape[0]

    def body(o_ref, gate_ref, g_ref, y_ref):
        ov, gv = o_ref[...], gate_ref[...]
        r = lax.rsqrt(jnp.mean(ov * ov, axis=-1, keepdims=True) + RMS_EPS)
        y_ref[...] = (ov * r * g_ref[...] * gv * _sigmoid(gv)).astype(BF16)

    blk = pl.BlockSpec((t, HEAD_A), lambda j: (0, j))
    return pl.pallas_call(
        body, name=name, grid=(N_HEADS_A,), in_specs=[blk, blk, pl.BlockSpec((1, HEAD_A), lambda j: (0, 0))],
        out_specs=blk, out_shape=jax.ShapeDtypeStruct((t, WIDTH_A), BF16),
        compiler_params=_params(("parallel",)))(o, gate, g)


def _onorm_bwd(o, gate, g, dy, name):
    t = o.shape[0]

    def body(o_ref, gate_ref, g_ref, dy_ref, do_ref, dgate_ref, dg_ref):
        @pl.when(pl.program_id(0) == 0)
        def _():
            dg_ref[...] = jnp.zeros_like(dg_ref)

        ov, gv, dyv = o_ref[...], gate_ref[...], dy_ref[...].astype(F32)
        r = lax.rsqrt(jnp.mean(ov * ov, axis=-1, keepdims=True) + RMS_EPS)
        oh = ov * r
        sg, dsg = _silu_and_grad(gv)
        dgate_ref[...] = (dyv * oh * g_ref[...] * dsg).astype(BF16)
        dn = dyv * sg
        dg_ref[...] += jnp.sum(dn * oh, axis=0, keepdims=True)
        dng = dn * g_ref[...]
        do_ref[...] = r * (dng - oh * jnp.mean(dng * oh, axis=-1, keepdims=True))

    blk = pl.BlockSpec((t, HEAD_A), lambda j: (0, j))
    vec = pl.BlockSpec((1, HEAD_A), lambda j: (0, 0))
    return pl.pallas_call(
        body, name=name, grid=(N_HEADS_A,), in_specs=[blk, blk, vec, blk], out_specs=[blk, blk, vec],
        out_shape=[jax.ShapeDtypeStruct((t, WIDTH_A), F32), jax.ShapeDtypeStruct((t, WIDTH_A), BF16),
                   jax.ShapeDtypeStruct((1, HEAD_A), F32)],
        compiler_params=_params(("arbitrary",)))(o, gate, g, dy)


def _cmul(ar, ai, br, bi):
    return ar * br - ai * bi, ar * bi + ai * br


def _scan_tables(ar, ai, reverse):
    p1 = (ar, ai)
    p2 = _cmul(*p1, *p1)
    p4 = _cmul(*p2, *p2)
    p8 = _cmul(*p4, *p4)
    p3 = _cmul(*p2, *p1)
    p5 = _cmul(*p4, *p1)
    p6 = _cmul(*p4, *p2)
    p7 = _cmul(*p4, *p3)
    pows = [p1, p2, p3, p4, p5, p6, p7, p8]
    rows = lax.broadcasted_iota(jnp.int32, (8, ar.shape[1]), 0)
    tr = jnp.zeros((8, ar.shape[1]), F32)
    ti = jnp.zeros((8, ar.shape[1]), F32)
    for r in range(8):
        pw = pows[7 - r] if reverse else pows[r]
        tr = jnp.where(rows == r, pw[0], tr)
        ti = jnp.where(rows == r, pw[1], ti)
    return p1, p2, p4, p8, tr, ti


def _tile_scan(xr, xi, p1, p2, p4, reverse):
    rows = lax.broadcasted_iota(jnp.int32, xr.shape, 0)
    for s, (pr, pi) in ((1, p1), (2, p2), (4, p4)):
        if reverse:
            keep = rows < 8 - s
            sr, si = pltpu.roll(xr, 8 - s, 0), pltpu.roll(xi, 8 - s, 0)
        else:
            keep = rows >= s
            sr, si = pltpu.roll(xr, s, 0), pltpu.roll(xi, s, 0)
        sr, si = jnp.where(keep, sr, 0.0), jnp.where(keep, si, 0.0)
        mr, mi = _cmul(pr, pi, sr, si)
        xr, xi = xr + mr, xi + mi
    return xr, xi


def _s5_scan_fwd(bu, a, name, tb=512, comm=None):
    t = bu.shape[0]
    cb = SCAN_CB
    nt = t // tb

    def body(b_ref, a_ref, x_ref, carry):
        @pl.when(pl.program_id(1) == 0)
        def _():
            carry[...] = jnp.zeros_like(carry)

        ar, ai = a_ref[:, 0:cb], a_ref[:, cb:2 * cb]
        p1, p2, p4, p8, tr, ti = _scan_tables(ar, ai, False)

        def step(j, c):
            cr, ci = c
            i = pl.multiple_of(j * 8, 8)
            xr, xi = _tile_scan(b_ref[pl.ds(i, 8), 0:cb], b_ref[pl.ds(i, 8), cb:2 * cb], p1, p2, p4, False)
            mr, mi = _cmul(tr, ti, cr, ci)
            xr, xi = xr + mr, xi + mi
            x_ref[pl.ds(i, 8), 0:cb] = xr
            x_ref[pl.ds(i, 8), cb:2 * cb] = xi
            return xr[7:8, :], xi[7:8, :]

        cr, ci = lax.fori_loop(0, tb // 8, step, (carry[0:1, :], carry[1:2, :]), unroll=2)
        carry[0:1, :] = cr
        carry[1:2, :] = ci

    blk = pl.BlockSpec((tb, 2 * cb), lambda j, i: (i, j))
    return _call(
        body, name=name, grid=(SSM_CH // cb, nt),
        in_specs=[blk, pl.BlockSpec((1, 2 * cb), lambda j, i: (0, j))], out_specs=blk,
        out_shape=jax.ShapeDtypeStruct((t, 2 * SSM_CH), F32), scratch_shapes=[pltpu.VMEM((8, cb), F32)],
        sem=("parallel", "arbitrary"), args=(bu, a), comm=comm)


def _s5_scan_bwd(dx, x, a, name, tb=512, comm=None):
    t = dx.shape[0]
    cb = SCAN_CB
    nt = t // tb
    nj = tb // 8

    def body(d_ref, x_ref, xp_ref, a_ref, l_ref, da_ref, carry, acc):
        tblk = pl.program_id(1)

        @pl.when(tblk == 0)
        def _():
            carry[...] = jnp.zeros_like(carry)
            acc[...] = jnp.zeros_like(acc)

        ar, ai = a_ref[:, 0:cb], a_ref[:, cb:2 * cb]
        p1, p2, p4, p8, tr, ti = _scan_tables(ar, -ai, True)
        rows = lax.broadcasted_iota(jnp.int32, (8, cb), 0)

        def step(jj, c):
            cr, ci, sr_acc, si_acc = c
            j = nj - 1 - jj
            i = pl.multiple_of(j * 8, 8)
            lr, li = _tile_scan(d_ref[pl.ds(i, 8), 0:cb], d_ref[pl.ds(i, 8), cb:2 * cb], p1, p2, p4, True)
            mr, mi = _cmul(tr, ti, cr, ci)
            lr, li = lr + mr, li + mi
            l_ref[pl.ds(i, 8), 0:cb] = lr
            l_ref[pl.ds(i, 8), cb:2 * cb] = li
            ip = pl.multiple_of(jnp.maximum(j - 1, 0) * 8, 8)
            prev_r = jnp.where(j > 0, x_ref[pl.ds(ip, 8), 0:cb], xp_ref[:, 0:cb])
            prev_i = jnp.where(j > 0, x_ref[pl.ds(ip, 8), cb:2 * cb], xp_ref[:, cb:2 * cb])
            edge = jnp.where(jnp.logical_and(j == 0, tblk == nt - 1), 0.0, 1.0)
            xs_r = jnp.where(rows == 0, pltpu.roll(prev_r, 1, 0) * edge, pltpu.roll(x_ref[pl.ds(i, 8), 0:cb], 1, 0))
            xs_i = jnp.where(rows == 0, pltpu.roll(prev_i, 1, 0) * edge, pltpu.roll(x_ref[pl.ds(i, 8), cb:2 * cb], 1, 0))
            sr_acc = sr_acc + lr * xs_r + li * xs_i
            si_acc = si_acc + li * xs_r - lr * xs_i
            return lr[0:1, :], li[0:1, :], sr_acc, si_acc

        cr, ci, sr_acc, si_acc = lax.fori_loop(
            0, nj, step, (carry[0:1, :], carry[1:2, :], acc[:, 0:cb], acc[:, cb:2 * cb]))
        carry[0:1, :] = cr
        carry[1:2, :] = ci
        acc[:, 0:cb] = sr_acc
        acc[:, cb:2 * cb] = si_acc

        @pl.when(tblk == nt - 1)
        def _():
            da_ref[...] = jnp.sum(acc[...], axis=0, keepdims=True)

    blk = pl.BlockSpec((tb, 2 * cb), lambda j, i: (nt - 1 - i, j))
    prev = pl.BlockSpec((8, 2 * cb), lambda j, i: (jnp.maximum((nt - 1 - i) * (tb // 8) - 1, 0), j))
    vec = pl.BlockSpec((1, 2 * cb), lambda j, i: (0, j))
    return _call(
        body, name=name, grid=(SSM_CH // cb, nt), in_specs=[blk, blk, prev, vec], out_specs=[blk, vec],
        out_shape=[jax.ShapeDtypeStruct((t, 2 * SSM_CH), F32), jax.ShapeDtypeStruct((1, 2 * SSM_CH), F32)],
        scratch_shapes=[pltpu.VMEM((8, cb), F32), pltpu.VMEM((8, 2 * cb), F32)],
        sem=("parallel", "arbitrary"), args=(dx, x, x, a), comm=comm)


def _glu_fwd(yc, u, dvec, wg, bg, name, tr=256):
    t = yc.shape[0]

    def body(yc_ref, u_ref, d_ref, w_ref, b_ref, yl_ref, yb_ref):
        yl = yc_ref[...] + d_ref[...] * u_ref[...]
        yl_ref[...] = yl
        yg, _ = _gelu_and_grad(yl)
        z = jnp.dot(yg.astype(BF16), w_ref[...], preferred_element_type=F32) + b_ref[...]
        yb_ref[...] = (yg * _sigmoid(z)).astype(BF16)

    blk = pl.BlockSpec((tr, SSM_WIDTH), lambda i: (i, 0))
    vec = pl.BlockSpec((1, SSM_WIDTH), lambda i: (0, 0))
    return pl.pallas_call(
        body, name=name, grid=(t // tr,),
        in_specs=[blk, blk, vec, pl.BlockSpec((SSM_WIDTH, SSM_WIDTH), lambda i: (0, 0)), vec],
        out_specs=[blk, blk],
        out_shape=[jax.ShapeDtypeStruct((t, SSM_WIDTH), F32), jax.ShapeDtypeStruct((t, SSM_WIDTH), BF16)],
        compiler_params=_params(("parallel",)))(yc, u, dvec, wg, bg)


def _glu_bwd(yl, u, dvec, wg, bg, dyb, name, tr=256):
    t = yl.shape[0]

    def body(yl_ref, u_ref, d_ref, w_ref, b_ref, dy_ref, dyl_ref, du_ref, dw_ref, db_ref, dd_ref):
        @pl.when(pl.program_id(0) == 0)
        def _():
            dw_ref[...] = jnp.zeros_like(dw_ref)
            db_ref[...] = jnp.zeros_like(db_ref)
            dd_ref[...] = jnp.zeros_like(dd_ref)

        ylv, dyv, wv = yl_ref[...], dy_ref[...].astype(F32), w_ref[...]
        yg, dgelu = _gelu_and_grad(ylv)
        ygb = yg.astype(BF16)
        z = jnp.dot(ygb, wv, preferred_element_type=F32) + b_ref[...]
        sg = _sigmoid(z)
        dz = dyv * yg * sg * (1.0 - sg)
        dzb = dz.astype(BF16)
        dyg = dyv * sg + lax.dot_general(dzb, wv, (((1,), (1,)), ((), ())), preferred_element_type=F32)
        dyl = dyg * dgelu
        dyl_ref[...] = dyl.astype(BF16)
        du_ref[...] = dyl * d_ref[...]
        dw_ref[...] += lax.dot_general(ygb, dzb, (((0,), (0,)), ((), ())), preferred_element_type=F32)
        db_ref[...] += jnp.sum(dz, axis=0, keepdims=True)
        dd_ref[...] += jnp.sum(dyl * u_ref[...], axis=0, keepdims=True)

    blk = pl.BlockSpec((tr, SSM_WIDTH), lambda i: (i, 0))
    vec = pl.BlockSpec((1, SSM_WIDTH), lambda i: (0, 0))
    wsp = pl.BlockSpec((SSM_WIDTH, SSM_WIDTH), lambda i: (0, 0))
    return pl.pallas_call(
        body, name=name, grid=(t // tr,), in_specs=[blk, blk, vec, wsp, vec, blk],
        out_specs=[blk, blk, wsp, vec, vec],
        out_shape=[jax.ShapeDtypeStruct((t, SSM_WIDTH), BF16), jax.ShapeDtypeStruct((t, SSM_WIDTH), F32),
                   jax.ShapeDtypeStruct((SSM_WIDTH, SSM_WIDTH), F32), jax.ShapeDtypeStruct((1, SSM_WIDTH), F32),
                   jax.ShapeDtypeStruct((1, SSM_WIDTH), F32)],
        compiler_params=_params(("arbitrary",)))(yl, u, dvec, wg, bg, dyb)


def _mesh_pos():
    return lax.axis_index("x"), lax.axis_index("y"), lax.axis_index("c")


def _device_index():
    x, y, c = _mesh_pos()
    return 4 * x + 2 * y + c


def _gather_comm(arrays):
    na = len(arrays)

    def own_copy(ins, outs, sems, ai):
        return pltpu.make_async_copy(ins[ai], outs[ai].at[_device_index()], sems[2].at[ai])

    def ctx(ins, outs, sems):
        send_sems, recv_sems = sems[:2]
        x, y, c = _mesh_pos()
        chips = [(1 - x, y), (x, 1 - y), (1 - x, 1 - y)]

        def copy(ai, kk, block, to, own=False):
            slot = outs[ai].at[4 * block[0] + 2 * block[1] + block[2]]
            return pltpu.make_async_remote_copy(
                src_ref=ins[ai] if own else slot, dst_ref=slot, send_sem=send_sems.at[ai, kk],
                recv_sem=recv_sems.at[ai, kk], device_id=to, device_id_type=MESH)

        return (x, y, c), (x, y, 1 - c), chips, c, copy

    def start(ins, outs, sems):
        me, sibling, chips, c, copy = ctx(ins, outs, sems)
        for ai in range(na):
            copy(ai, 0, me, sibling, own=True).start()
            for j, chip in enumerate(chips):
                copy(ai, 1 + j, me, (*chip, c), own=True).start()
        for ai in range(na):
            own_copy(ins, outs, sems, ai).start()

    def mid(ins, outs, sems):
        me, sibling, chips, c, copy = ctx(ins, outs, sems)
        for ai in range(na):
            for j, chip in enumerate(chips):
                copy(ai, 1 + j, (*chip, c), me).wait_recv()
                copy(ai, 4 + j, (*chip, c), sibling).start()

    def end(ins, outs, sems):
        me, sibling, chips, c, copy = ctx(ins, outs, sems)
        for ai in range(na):
            copy(ai, 0, sibling, me).wait_recv()
            copy(ai, 0, me, sibling, own=True).wait_send()
            for j, chip in enumerate(chips):
                copy(ai, 4 + j, (*chip, 1 - c), me).wait_recv()
                copy(ai, 1 + j, me, (*chip, c), own=True).wait_send()
                copy(ai, 4 + j, (*chip, c), sibling).wait_send()
            own_copy(ins, outs, sems, ai).wait()

    return Comm(arrays, [jax.ShapeDtypeStruct((N_DEV,) + a.shape, a.dtype) for a in arrays],
                [pltpu.SemaphoreType.DMA((na, 7)), pltpu.SemaphoreType.DMA((na, 7)), pltpu.SemaphoreType.DMA((na,))],
                start, end, mid)


def _sequencer_gather(arrays, name, collective_id):
    comm = _gather_comm(arrays)
    na = len(arrays)

    def body(*refs):
        ins, outs, sems = refs[:na], refs[na:2 * na], refs[2 * na:]
        x, y, c = _mesh_pos()
        peers = [(x, y, 1 - c), (1 - x, y, c), (x, 1 - y, c), (1 - x, 1 - y, c)]
        barrier = pltpu.get_barrier_semaphore()
        for peer in peers:
            pl.semaphore_signal(barrier, inc=1, device_id=peer, device_id_type=MESH)
        pl.semaphore_wait(barrier, len(peers))
        comm.start(ins, outs, sems)
        comm.mid(ins, outs, sems)
        comm.end(ins, outs, sems)

    return list(pl.kernel(
        body, out_type=tuple(comm.out_shapes), mesh=plsc.ScalarSubcoreMesh(axis_name="sequencer", num_cores=1),
        name=name, scratch_types=tuple(comm.sems),
        compiler_params=pltpu.CompilerParams(collective_id=collective_id))(*arrays))


def _sequencer_exchange(comm, peers_of, name, collective_id):
    na = len(comm.inputs)

    def body(*refs):
        ins, outs, sems = refs[:na], refs[na:na + len(comm.out_shapes)], refs[na + len(comm.out_shapes):]
        peers = peers_of(*_mesh_pos())
        barrier = pltpu.get_barrier_semaphore()
        for peer in peers:
            pl.semaphore_signal(barrier, inc=1, device_id=peer, device_id_type=MESH)
        pl.semaphore_wait(barrier, len(peers))
        comm.start(ins, outs, sems)
        comm.end(ins, outs, sems)

    return list(pl.kernel(
        body, out_type=tuple(comm.out_shapes), mesh=plsc.ScalarSubcoreMesh(axis_name="sequencer", num_cores=1),
        name=name, scratch_types=tuple(comm.sems),
        compiler_params=pltpu.CompilerParams(collective_id=collective_id))(*comm.inputs))


SIBLING_SWAP_ID, CHIP_EXCHANGE_ID = 9, 10


def _sequencer_swap(arrays, name):
    return _sequencer_exchange(_swap_comm(arrays), lambda x, y, c: [(x, y, 1 - c)], name, SIBLING_SWAP_ID)[0]


def _sequencer_chips(send, name):
    return _sequencer_exchange(_chips_comm(send), lambda x, y, c: [(1 - x, y, c), (x, 1 - y, c), (1 - x, 1 - y, c)],
                               name, CHIP_EXCHANGE_ID)[0]


def _swap_comm(arrays):
    na = len(arrays)
    offs = np.concatenate([[0], np.cumsum([a.shape[1] for a in arrays])]).astype(int)

    def copies(ins, outs, sems):
        x, y, c = _mesh_pos()
        return [pltpu.make_async_remote_copy(
            src_ref=ins[ai].at[2 * k + 1 - c], dst_ref=outs[0].at[k, pl.ds(int(offs[ai]), arrays[ai].shape[1])],
            send_sem=sems[0].at[ai, k], recv_sem=sems[1].at[ai, k], device_id=(x, y, 1 - c), device_id_type=MESH)
            for ai in range(na) for k in range(4)]

    def start(ins, outs, sems):
        for cp in copies(ins, outs, sems):
            cp.start()

    def end(ins, outs, sems):
        for cp in copies(ins, outs, sems):
            cp.wait()

    return Comm(arrays, [jax.ShapeDtypeStruct((4, int(offs[-1]), PACK_COLS), arrays[0].dtype)],
                [pltpu.SemaphoreType.DMA((na, 4)), pltpu.SemaphoreType.DMA((na, 4))], start, end)


def _chips_comm(send):
    def copies(ins, outs, sems):
        x, y, c = _mesh_pos()
        chips = [(1 - x, y), (x, 1 - y), (1 - x, 1 - y)]
        return [pltpu.make_async_remote_copy(
            src_ref=ins[0].at[2 * cx + cy], dst_ref=outs[0].at[j], send_sem=sems[0].at[j], recv_sem=sems[1].at[j],
            device_id=(cx, cy, c), device_id_type=MESH) for j, (cx, cy) in enumerate(chips)]

    def start(ins, outs, sems):
        for cp in copies(ins, outs, sems):
            cp.start()

    def end(ins, outs, sems):
        for cp in copies(ins, outs, sems):
            cp.wait()

    return Comm([send], [jax.ShapeDtypeStruct((3,) + send.shape[1:], send.dtype)],
                [pltpu.SemaphoreType.DMA((3,)), pltpu.SemaphoreType.DMA((3,))], start, end)


def _all_gather(arrays, name):
    na = len(arrays)

    def body(*refs):
        ins, outs = refs[:na], refs[na:2 * na]
        send_sems, recv_sems, local_sems = refs[2 * na:]
        x, y, c = _mesh_pos()
        me, sibling = (x, y, c), (x, y, 1 - c)
        chips = [(1 - x, y), (x, 1 - y), (1 - x, 1 - y)]
        waits = []
        for ai in range(na):
            in_ref, out_ref = ins[ai], outs[ai]

            def slot(px, py, pc, out_ref=out_ref):
                return out_ref.at[4 * px + 2 * py + pc]

            def copy(kk, block, to, src=None, ai=ai, slot=slot):
                return pltpu.make_async_remote_copy(
                    src_ref=slot(*block) if src is None else src, dst_ref=slot(*block),
                    send_sem=send_sems.at[ai, kk], recv_sem=recv_sems.at[ai, kk], device_id=to, device_id_type=MESH)

            mine = pltpu.make_async_copy(in_ref, slot(*me), local_sems.at[ai])
            mine.start()
            first = [copy(0, me, sibling, src=in_ref)]
            first += [copy(1 + j, me, (*chip, c), src=in_ref) for j, chip in enumerate(chips)]
            for cp in first:
                cp.start()
            waits.append((copy, mine, first))
        sends = []
        for ai in range(na):
            copy, mine, first = waits[ai]
            passed = [copy(4 + j, (*chip, c), sibling) for j, chip in enumerate(chips)]
            for j, chip in enumerate(chips):
                copy(1 + j, (*chip, c), me).wait_recv()
                passed[j].start()
            sends.append(passed)
        for ai in range(na):
            copy, mine, first = waits[ai]
            copy(0, sibling, me).wait_recv()
            for j, chip in enumerate(chips):
                copy(4 + j, (*chip, 1 - c), me).wait_recv()
            for cp in first + sends[ai]:
                cp.wait_send()
            mine.wait()

    any_spec = pl.BlockSpec(memory_space=pl.ANY)
    return pl.pallas_call(
        body, name=name, in_specs=[any_spec] * na, out_specs=[any_spec] * na,
        out_shape=[jax.ShapeDtypeStruct((N_DEV,) + a.shape, a.dtype) for a in arrays],
        scratch_shapes=[pltpu.SemaphoreType.DMA((na, 7)), pltpu.SemaphoreType.DMA((na, 7)),
                        pltpu.SemaphoreType.DMA((na,))],
        compiler_params=pltpu.CompilerParams(has_side_effects=True))(*arrays)


def _swap_sibling(arrays, name):
    na = len(arrays)
    offs = np.concatenate([[0], np.cumsum([a.shape[1] for a in arrays])]).astype(int)
    rows = int(offs[-1])

    def body(*refs):
        ins, recv_ref = refs[:na], refs[na]
        send_sems, recv_sems = refs[na + 1:]
        x, y, c = _mesh_pos()
        started = []
        for ai in range(na):
            span = pl.ds(int(offs[ai]), arrays[ai].shape[1])
            for k in range(4):
                remote = pltpu.make_async_remote_copy(
                    src_ref=ins[ai].at[2 * k + 1 - c], dst_ref=recv_ref.at[k, span], send_sem=send_sems.at[ai, k],
                    recv_sem=recv_sems.at[ai, k], device_id=(x, y, 1 - c), device_id_type=MESH)
                remote.start()
                started.append(remote)
        for remote in started:
            remote.wait()

    any_spec = pl.BlockSpec(memory_space=pl.ANY)
    return pl.pallas_call(
        body, name=name, in_specs=[any_spec] * na, out_specs=any_spec,
        out_shape=jax.ShapeDtypeStruct((4, rows, PACK_COLS), arrays[0].dtype),
        scratch_shapes=[pltpu.SemaphoreType.DMA((na, 4)), pltpu.SemaphoreType.DMA((na, 4))])(*arrays)


def _exchange_chips(send, name):
    def body(s_ref, o_ref, send_sems, recv_sems):
        x, y, c = _mesh_pos()
        chips = [(1 - x, y), (x, 1 - y), (1 - x, 1 - y)]
        cps = [pltpu.make_async_remote_copy(
            src_ref=s_ref.at[2 * cx + cy], dst_ref=o_ref.at[j], send_sem=send_sems.at[j], recv_sem=recv_sems.at[j],
            device_id=(cx, cy, c), device_id_type=MESH) for j, (cx, cy) in enumerate(chips)]
        for cp in cps:
            cp.start()
        for cp in cps:
            cp.wait()

    any_spec = pl.BlockSpec(memory_space=pl.ANY)
    return pl.pallas_call(
        body, name=name, in_specs=[any_spec], out_specs=any_spec,
        out_shape=jax.ShapeDtypeStruct((3,) + send.shape[1:], send.dtype),
        scratch_shapes=[pltpu.SemaphoreType.DMA((3,)), pltpu.SemaphoreType.DMA((3,))])(send)


def _pair_sum(keep, recv, name, tr=464):
    nchip, rows, cols = keep.shape

    def body(g_ref, r_ref, o_ref):
        o_ref[...] = (g_ref[...].astype(F32) + r_ref[...].astype(F32)).astype(BF16)

    blk = pl.BlockSpec((1, tr, cols), lambda k, i: (k, i, 0))
    return pl.pallas_call(
        body, name=name, grid=(nchip, rows // tr), in_specs=[blk, blk], out_specs=blk,
        out_shape=jax.ShapeDtypeStruct((nchip, rows, cols), BF16),
        compiler_params=_params(("parallel", "parallel")))(keep, recv)


def _pair_sum_pieces(pieces, recv, name, tr):
    _, rows, cols = pieces.shape
    core = lax.axis_index("c").astype(jnp.int32).reshape(1)

    def body(c_ref, g_ref, r_ref, o_ref):
        del c_ref
        o_ref[...] = (g_ref[...].astype(F32) + r_ref[...].astype(F32)).astype(BF16)

    grid_spec = pltpu.PrefetchScalarGridSpec(
        num_scalar_prefetch=1, grid=(4, rows // tr),
        in_specs=[pl.BlockSpec((1, tr, cols), lambda k, i, c_ref: (2 * k + c_ref[0], i, 0)),
                  pl.BlockSpec((1, tr, cols), lambda k, i, c_ref: (k, i, 0))],
        out_specs=pl.BlockSpec((1, tr, cols), lambda k, i, c_ref: (k, i, 0)))
    return pl.pallas_call(
        body, name=name, grid_spec=grid_spec, out_shape=jax.ShapeDtypeStruct((4, rows, cols), BF16),
        compiler_params=_params(("parallel", "parallel")))(core, pieces, recv)


def _chip_sum(own, others, name, tr=464):
    _, rows, cols = own.shape
    chip = (2 * lax.axis_index("x") + lax.axis_index("y")).astype(jnp.int32).reshape(1)

    def body(chip_ref, own_ref, oth_ref, o_ref):
        del chip_ref
        acc = own_ref[0].astype(F32)
        for j in range(3):
            acc = acc + oth_ref[j].astype(F32)
        o_ref[...] = acc

    grid_spec = pltpu.PrefetchScalarGridSpec(
        num_scalar_prefetch=1, grid=(rows // tr,),
        in_specs=[pl.BlockSpec((1, tr, cols), lambda i, chip_ref: (chip_ref[0], i, 0)),
                  pl.BlockSpec((3, tr, cols), lambda i, chip_ref: (0, i, 0))],
        out_specs=pl.BlockSpec((tr, cols), lambda i, chip_ref: (i, 0)))
    return pl.pallas_call(
        body, name=name, grid_spec=grid_spec, out_shape=jax.ShapeDtypeStruct((rows, cols), F32),
        compiler_params=_params(("parallel",)))(chip, own, others)


def _sum_leading(parts, name, tr=464):
    nparts, rows, cols = parts.shape
    tr = tr if rows % tr == 0 else rows

    def body(p_ref, o_ref):
        acc = p_ref[0].astype(F32)
        for i in range(1, nparts):
            acc = acc + p_ref[i].astype(F32)
        o_ref[...] = acc

    return pl.pallas_call(
        body, name=name, grid=(rows // tr,),
        in_specs=[pl.BlockSpec((nparts, tr, cols), lambda i: (0, i, 0))],
        out_specs=pl.BlockSpec((tr, cols), lambda i: (i, 0)), out_shape=jax.ShapeDtypeStruct((rows, cols), F32),
        compiler_params=_params(("parallel",)))(parts)


def _adamw(w, g, m, v, name, comm=None):
    shape = w.shape
    cols = shape[-1]
    lead = shape[0] if len(shape) >= 3 else 1
    rows = int(np.prod(shape[:-1])) // lead if len(shape) > 1 else 1
    w2, g2, m2, v2 = (a.reshape(lead, rows, cols) for a in (w, g, m, v))
    tr = rows
    for cand in (512, 256, 128, 64, 32, 16, 8):
        if rows % cand == 0 and rows > cand:
            tr = cand
            break
    bc1, bc2 = 1.0 - ADAM_B1 ** ADAM_STEP, 1.0 - ADAM_B2 ** ADAM_STEP

    def body(w_ref, g_ref, m_ref, v_ref, d_ref, nm_ref, nv_ref):
        gv = g_ref[...]
        nm = ADAM_B1 * m_ref[...] + (1.0 - ADAM_B1) * gv
        nv = ADAM_B2 * v_ref[...] + (1.0 - ADAM_B2) * (gv * gv)
        nm_ref[...] = nm
        nv_ref[...] = nv
        d_ref[...] = -ADAM_LR * ((nm / bc1) / (jnp.sqrt(nv / bc2) + ADAM_EPS) + ADAM_WD * w_ref[...])

    blk = pl.BlockSpec((1, tr, cols), lambda l, i: (l, i, 0))
    res = _call(body, name=name, grid=(lead, rows // tr), in_specs=[blk] * 4, out_specs=[blk] * 3,
                out_shape=[jax.ShapeDtypeStruct((lead, rows, cols), F32)] * 3, sem=("parallel", "parallel"),
                args=(w2, g2, m2, v2), comm=comm)
    outs, couts = res if comm is not None else (res, None)
    outs = tuple(o.reshape(shape) for o in outs)
    return outs if comm is None else (outs, couts)


WEIGHT_NAMES = ['norm_mix_g', 'norm_xa_g', 'norm_ffn_g', 'norm_mem_g', 'norm_final_g', 'w_in_ab', 'conv_qkv_a',
                'a_log_a', 'dt_bias_a', 'onorm_g_a', 'ssm_lambda_re', 'ssm_lambda_im', 'ssm_b_re', 'ssm_b_im',
                'ssm_c_re', 'ssm_c_im', 'ssm_d', 'ssm_log_dt', 'w_glu_b', 'b_glu_b', 'w_out_ab', 'pool_w',
                'pool_scale', 'xa_wq', 'xa_wkv', 'xa_wo', 'ffn_w_up', 'ffn_conv', 'ffn_w_down']
BIG_SHARDED = {'w_in_ab': ((1, 1024, 2568), 2), 'w_glu_b': ((1, 512, 512), 1), 'w_out_ab': ((1, 1024, 1024), 1),
               'pool_w': ((1, 4, 256, 256), 2), 'xa_wq': ((2, 1024, 1024), 1), 'xa_wkv': ((2, 1024, 2048), 2),
               'xa_wo': ((2, 1024, 1024), 1), 'ffn_w_up': ((2, 1024, 5632), 2), 'ffn_w_down': ((2, 2816, 1024), 1)}
SMALL_SHARDED = {'conv_qkv_a': ((1, 4, 1536), 2), 'pool_scale': ((1, 1024), 1), 'ffn_conv': ((2, 3, 5632), 2)}
REPLICATED = {'norm_mix_g': (2, 1024), 'norm_xa_g': (2, 1024), 'norm_ffn_g': (2, 1024), 'norm_mem_g': (1024,),
              'norm_final_g': (1024,), 'a_log_a': (1, 4), 'dt_bias_a': (1, 4), 'onorm_g_a': (1, 128),
              'ssm_lambda_re': (1, 32, 64), 'ssm_lambda_im': (1, 32, 64), 'ssm_b_re': (1, 32, 64, 16),
              'ssm_b_im': (1, 32, 64, 16), 'ssm_c_re': (1, 32, 16, 64), 'ssm_c_im': (1, 32, 16, 64),
              'ssm_d': (1, 32, 16), 'ssm_log_dt': (1, 32), 'b_glu_b': (1, 512)}
PACK_ROW_ALIGN = 8


def _shard_shape(shape, axis):
    return tuple(s // N_DEV if i == axis else s for i, s in enumerate(shape))


def _round_up(n, m):
    return (n + m - 1) // m * m


def _pack(arrays):
    total = sum(int(np.prod(a.shape)) for a in arrays)
    padded = _round_up(total, PACK_COLS * PACK_ROW_ALIGN)
    parts = [a.astype(F32).reshape(-1) for a in arrays]
    if padded != total:
        parts.append(jnp.zeros((padded - total,), F32))
    return jnp.concatenate(parts).reshape(padded // PACK_COLS, PACK_COLS)


def _unpack(packed, shapes):
    flat, out, off = packed.reshape(-1), [], 0
    for shape in shapes:
        size = int(np.prod(shape))
        out.append(flat[off:off + size].reshape(shape))
        off += size
    return out


def _split_shards(full, axis):
    shape = full.shape
    s = shape[axis] // N_DEV
    a = full.reshape(shape[:axis] + (N_DEV, s) + shape[axis + 1:])
    return jnp.moveaxis(a, axis, 0).reshape(N_DEV, -1)


def _merge_shards(pieces, shape, axis):
    sh = _shard_shape(shape, axis)
    a = pieces.reshape((N_DEV,) + sh)
    a = jnp.moveaxis(a, 0, axis)
    return a.reshape(shape)


_SCAN_NB = SSM_CH // SCAN_CB


def _to_scan_layout(m, axis):
    shape = m.shape
    m = m.reshape(shape[:axis] + (2, _SCAN_NB, SCAN_CB) + shape[axis + 1:])
    return jnp.swapaxes(m, axis, axis + 1).reshape(shape)


def _from_scan_layout(m, axis):
    shape = m.shape
    m = m.reshape(shape[:axis] + (_SCAN_NB, 2, SCAN_CB) + shape[axis + 1:])
    return jnp.swapaxes(m, axis, axis + 1).reshape(shape)


def _s5_discretise(lam_re, lam_im, b_re, b_im, log_dt):
    dt = jnp.exp(log_dt)[:, None]
    mag = jnp.exp(lam_re * dt)
    ang = lam_im * dt
    lb_re, lb_im = mag * jnp.cos(ang), mag * jnp.sin(ang)
    den = lam_re * lam_re + lam_im * lam_im
    nr, ni = lb_re - 1.0, lb_im
    coef_re = (nr * lam_re + ni * lam_im) / den
    coef_im = (ni * lam_re - nr * lam_im) / den
    bb_re = coef_re[..., None] * b_re - coef_im[..., None] * b_im
    bb_im = coef_re[..., None] * b_im + coef_im[..., None] * b_re
    return lb_re, lb_im, bb_re, bb_im


_GROUPS_PER_BLOCK = N_GROUPS // _SCAN_NB
_U_BLOCK = _GROUPS_PER_BLOCK * SSM_GROUP


def _s5_matrices(lb_re, lb_im, bb_re, bb_im, c_re, c_im):
    eye = jnp.eye(_GROUPS_PER_BLOCK, dtype=F32)
    blocked = lambda m: m.reshape((_SCAN_NB, _GROUPS_PER_BLOCK) + m.shape[1:])
    bmat = lambda bb: jnp.einsum('jgph,gk->jghkp', blocked(bb), eye).reshape(_SCAN_NB, _U_BLOCK, SCAN_CB)
    cmat = lambda cc: jnp.einsum('jghp,gk->jkpgh', blocked(cc), eye).reshape(_SCAN_NB, SCAN_CB, _U_BLOCK)
    b_in = jnp.concatenate([bmat(bb_re), bmat(bb_im)], axis=2)
    c_out = jnp.concatenate([cmat(c_re), -cmat(c_im)], axis=1)
    a_row = _to_scan_layout(jnp.concatenate([lb_re.reshape(1, SSM_CH), lb_im.reshape(1, SSM_CH)], axis=1), 1)
    return b_in, c_out, a_row


def _s5_matrix_grads(db_in, dc_out, da_row):
    da_nat = _from_scan_layout(da_row, 1)
    eye = jnp.eye(_GROUPS_PER_BLOCK, dtype=F32)
    nb, gb = _SCAN_NB, _GROUPS_PER_BLOCK
    bgrad = lambda m: jnp.einsum('jghkp,gk->jgph', m.reshape(nb, gb, SSM_GROUP, gb, SSM_STATE), eye
                                 ).reshape(N_GROUPS, SSM_STATE, SSM_GROUP)
    cgrad = lambda m: jnp.einsum('jkpgh,gk->jghp', m.reshape(nb, gb, SSM_STATE, gb, SSM_GROUP), eye
                                 ).reshape(N_GROUPS, SSM_GROUP, SSM_STATE)
    dbb_re, dbb_im = bgrad(db_in[:, :, :SCAN_CB]), bgrad(db_in[:, :, SCAN_CB:])
    dc_re, dc_im = cgrad(dc_out[:, :SCAN_CB]), -cgrad(dc_out[:, SCAN_CB:])
    dlb_re = da_nat[0, :SSM_CH].reshape(N_GROUPS, SSM_STATE)
    dlb_im = da_nat[0, SSM_CH:].reshape(N_GROUPS, SSM_STATE)
    return dlb_re, dlb_im, dbb_re, dbb_im, dc_re, dc_im


def _as_pieces(a):
    return a.reshape(N_DEV, a.shape[0] // N_DEV, a.shape[1])


def _hybrid_fwd(xn, x, wts, p, weights, riders):
    sv = {}
    hq = _mm(xn, wts['w_qkv_t'], "nt", "l0_in_qkv")
    gate = _mm(xn, wts['w_gate_t'], "nt", "l0_in_gate")
    ba = _mm(xn, wts['w_ba_t'], "nt", "l0_in_ba")
    u = _mm(xn, wts['w_u_t'], "nt", "l0_in_u")
    conv = p['conv_qkv']
    q = _qkv_pre_fwd(hq, conv, 0, 4, True, HEAD_A ** -0.5, "l0_q_pre")
    k = _qkv_pre_fwd(hq, conv, 4, 4, True, 1.0, "l0_k_pre")
    v = _qkv_pre_fwd(hq, conv, 8, 4, False, 1.0, "l0_v_pre")
    gates = _gates_fwd(ba, p['arow'], p['brow'], "l0_gates")
    o, tm_all, s_all = riders.run("l0_gdr_fwd", _gdr_fwd, q, k, v, gates)
    wts['w_glu'], wts['w_out'] = weights.full['w_glu'], weights.full['w_out']
    y_a = _onorm_fwd(o, gate, p['onorm_g'], "l0_onorm")
    bu = riders.run("l0_s5_bu", _mm_bd, u, p['b_in'], "nn")
    xs = riders.run("l0_s5_scan", _s5_scan_fwd, bu, p['a_row'])
    weights.gather_by_sequencer(GATHER_LAYER1, xs, "gather_layer1", GATHER_LAYER1_ID)
    yc = riders.run("l0_s5_cx", _mm_bd, xs, p['c_out'], "nn")
    yl, y_b = _glu_fwd(yc, u, p['d_row'], wts['w_glu'], p['b_glu'], "l0_glu")
    mixed = jnp.concatenate([y_a, y_b], axis=1)
    x1 = _mm(mixed, wts['w_out'], "nn", "l0_out", res=x)
    sv.update(hq=hq, gate=gate, ba=ba, u=u, q=q, k=k, v=v, gb=gates, o=o, tm=tm_all, s=s_all, xs=xs, yl=yl, mixed=mixed)
    return x1, sv


def _hybrid_bwd(dx1, xn, wts, p, sv, riders):
    gr = {}
    dmixed = _mm(dx1, wts['w_out'], "nt", "l0_out_dx", out_dtype=BF16)
    riders.grad('w_out', _as_pieces(_mm(sv['mixed'], dx1, "tn", "l0_out_dw", out_dtype=BF16)))
    dya, dyb = dmixed[:, :WIDTH_A], dmixed[:, WIDTH_A:]
    dyl, du_direct, dw_glu, gr['b_glu_b'], dd = _glu_bwd(
        sv['yl'], sv['u'], p['d_row'], wts['w_glu'], p['b_glu'], dyb, "l0_glu_bwd")
    riders.grad('w_glu', dw_glu.astype(BF16).reshape(N_DEV, -1, PACK_COLS))
    dxs = riders.run("l0_s5_cx_dx", _mm_bd, dyl, p['c_out'], "nt")
    dc_out = _mm_bd(sv['xs'], dyl, "tn", "l0_s5_cx_dw")
    lam, da_row = riders.run("l0_s5_scan_bwd", _s5_scan_bwd, dxs, sv['xs'], p['a_row'])
    du = _mm_bd(lam, p['b_in'], "nt", "l0_s5_bu_dx", res=du_direct, out_dtype=BF16)
    db_in = _mm_bd(sv['u'], lam, "tn", "l0_s5_bu_dw")
    gr['s5'] = (db_in, dc_out, da_row, dd)
    do, dgate, gr['onorm_g_a'] = _onorm_bwd(sv['o'], sv['gate'], p['onorm_g'], dya, "l0_onorm_bwd")
    dq, dk, dv, dgb = riders.run("l0_gdr_bwd", _gdr_bwd, sv['q'], sv['k'], sv['v'], sv['gb'], sv['tm'], sv['s'], do)
    conv = p['conv_qkv']
    dhq_q, dcw_q = _qkv_pre_bwd(sv['hq'], conv, dq, 0, 4, True, HEAD_A ** -0.5, "l0_q_pre_bwd")
    dhq_k, dcw_k = _qkv_pre_bwd(sv['hq'], conv, dk, 4, 4, True, 1.0, "l0_k_pre_bwd")
    dhq_v, dcw_v = _qkv_pre_bwd(sv['hq'], conv, dv, 8, 4, False, 1.0, "l0_v_pre_bwd")
    gr['conv_qkv_a'] = jnp.concatenate([dcw_q, dcw_k, dcw_v], axis=1)
    dhq = jnp.concatenate([dhq_q, dhq_k, dhq_v], axis=1)
    dba, da_log, ddt_bias = _gates_bwd(sv['ba'], p['arow'], p['brow'], dgb, "l0_gates_bwd")
    gr['a_log_a'], gr['dt_bias_a'] = da_log[:, 4:8], ddt_bias[:, 4:8]
    dw_qkv_t = _mm(dhq, xn, "tn", "l0_in_qkv_dw", out_dtype=BF16)
    dw_gate_t = _mm(dgate, xn, "tn", "l0_in_gate_dw", out_dtype=BF16)
    dw_ba_t = _mm(dba, xn, "tn", "l0_in_ba_dw", out_dtype=BF16)
    dw_u_t = _mm(du, xn, "tn", "l0_in_u_dw", out_dtype=BF16)
    dw_in_t = _as_pieces(jnp.concatenate([dw_qkv_t, dw_gate_t, dw_ba_t[:8], dw_u_t], axis=0))
    riders.grad('w_in_t', jnp.concatenate(
        [dw_in_t, jnp.zeros((N_DEV, dict(PIECES)['w_in_t'] - W_IN_PIECE, D_MODEL), BF16)], axis=1))
    dxn = riders.run("l0_in_qkv_dx", _mm, dhq, wts['w_qkv_t'], "nn")
    dxn = _mm(dgate, wts['w_gate_t'], "nn", "l0_in_gate_dx", res=dxn)
    dxn = _mm(dba, wts['w_ba_t'], "nn", "l0_in_ba_dx", res=dxn)
    dxn = riders.run("l0_in_u_dx", _mm, du, wts['w_u_t'], "nn", res=dxn)
    return dxn, gr


def _xa_fwd(x1, g, mem_n, wq, wkv_t, wo, tag, riders):
    xq = _rms_fwd(x1, g, BF16, tag + "_norm")
    q = _mm(xq, wq, "nn", tag + "_q", out_dtype=BF16)
    kv = _mm(mem_n, wkv_t, "nt", tag + "_kv", out_dtype=BF16)
    o = riders.run(tag + "_attn", _attn_fwd, q, kv)
    x2 = _mm(o, wo, "nn", tag + "_o", res=x1)
    return x2, dict(xq=xq, q=q, kv=kv, o=o)


def _xa_bwd(dx2, x1, g, mem_n, wq, wkv_t, wo, sv, tag, layer, riders):
    do = _mm(dx2, wo, "nt", tag + "_o_dx", out_dtype=BF16)
    riders.grad('wo%d' % layer, _as_pieces(_mm(sv['o'], dx2, "tn", tag + "_o_dw", out_dtype=BF16)))
    dq, dk, dv = _attn_bwd(sv['q'], sv['kv'], do, tag + "_attn_bwd")
    dkv = jnp.concatenate([dk, dv], axis=1).astype(BF16)
    dxq = _mm(dq, wq, "nt", tag + "_q_dx")
    riders.grad('wq%d' % layer, _as_pieces(_mm(sv['xq'], dq, "tn", tag + "_q_dw", out_dtype=BF16)))
    dmem_n = _mm(dkv, wkv_t, "nn", tag + "_kv_dx")
    riders.grad('wkv_t%d' % layer, _as_pieces(_mm(dkv, mem_n, "tn", tag + "_kv_dw", out_dtype=BF16)))
    dx1, dg = riders.run(tag + "_norm_bwd", _rms_bwd, x1, g, dxq, dx2)
    return dx1, dmem_n, dg


def _ffn_fwd(x2, g, w_up_t, conv, w_down, tag, riders):
    xf = _rms_fwd(x2, g, BF16, tag + "_norm")
    h = riders.run(tag + "_up", _mm, xf, w_up_t, "nt")
    a = riders.run(tag + "_act", _ffn_act_fwd, h, conv)
    x3 = _mm(a, w_down, "nn", tag + "_down", res=x2)
    return x3, dict(xf=xf, h=h, a=a)


def _ffn_bwd(dx3, x2, g, w_up_t, conv, w_down, sv, tag, layer, riders):
    da = _mm(dx3, w_down, "nt", tag + "_down_dx")
    riders.grad('down%d' % layer, _as_pieces(_mm(sv['a'], dx3, "tn", tag + "_down_dw", out_dtype=BF16)))
    dh, dconv = riders.run(tag + "_act_bwd", _ffn_act_bwd, sv['h'], conv, da)
    dxf = riders.run(tag + "_up_dx", _mm, dh, w_up_t, "nn")
    dw_up_t = riders.run(tag + "_up_dw", _mm, dh, sv['xf'], "tn", out_dtype=BF16)
    riders.grad('up_t%d' % layer, _as_pieces(dw_up_t))
    dx2, dg = riders.run(tag + "_norm_bwd", _rms_bwd, x2, g, dxf, dx3)
    return dx2, dconv, dg


BIG_NAMES, SMALL_NAMES, REP_NAMES = list(BIG_SHARDED), list(SMALL_SHARDED), list(REPLICATED)
BIG_SIZES = [int(np.prod(_shard_shape(*BIG_SHARDED[n]))) for n in BIG_NAMES]
SMALL_SIZES = [int(np.prod(_shard_shape(*SMALL_SHARDED[n]))) for n in SMALL_NAMES]


PIECES = [('w_in_t', 384), ('w_glu', 32), ('w_out', 128), ('pool_w', 32), ('wq0', 128), ('wq1', 128),
          ('wkv_t0', 256), ('wkv_t1', 256), ('wo0', 128), ('wo1', 128), ('up_t0', 704), ('up_t1', 704),
          ('down0', 352), ('down1', 352)]
PIECE_OFFS = dict(zip([k for k, _ in PIECES], np.concatenate([[0], np.cumsum([r for _, r in PIECES])[:-1]]).tolist()))
W_IN_ROWS = 4 * WIDTH_A + 2 * N_HEADS_A + SSM_WIDTH
W_IN_PIECE = W_IN_ROWS // N_DEV


def _row_tile(rows):
    return max(t for t in range(16, min(rows, 512) + 1, 16) if rows % t == 0)


class _Riders:
    def __init__(self):
        self.waiting = {}
        self.deferred = {}
        self.grads = {}
        self.groups = []
        self.reduced = {}

    def add(self, host, comm, then):
        self.waiting.setdefault(host, []).append((comm, then))

    def after(self, marker, then):
        self.deferred.setdefault(marker, []).append(then)

    def mark(self, name, out=None):
        for cont in self.deferred.pop(name, []):
            step = cont()
            if step is not None:
                values, then = step
                out, values = lax.optimization_barrier((out, values))
                then(values)
        return out

    def run(self, name, fn, *args, **kw):
        riders = self.waiting.pop(name, [])
        if not riders:
            out = fn(*args, name=name, **kw)
        else:
            out, couts = fn(*args, name=name, comm=[c for c, _ in riders], **kw)
            for (_, then), got in zip(riders, couts):
                then(got)
        return self.mark(name, out)

    def grad(self, key, pieces):
        self.grads[key] = pieces
        for group in [g for g in self.groups if all(k in self.grads for k in g[1])]:
            self.groups.remove(group)
            self._reduce(*group)

    def _reduce(self, name, keys, pair_marker, sum_marker):
        arrays = [self.grads[k] for k in keys]
        rows = sum(a.shape[1] for a in arrays)
        tile = _row_tile(rows)
        from_sibling = _sequencer_swap(arrays, name + "_to_sibling")

        def after_swap():
            if len(arrays) == 1:
                chip_sums = _pair_sum_pieces(arrays[0], from_sibling, name + "_pair_sum", tr=tile)
            else:
                core = lax.axis_index("c")
                keep = jnp.concatenate(
                    [lax.dynamic_index_in_dim(a.reshape(4, 2, a.shape[1], PACK_COLS), core, 1, keepdims=False)
                     for a in arrays], axis=1)
                chip_sums = _pair_sum(keep, from_sibling, name + "_pair_sum", tr=tile)

            def exchange_among_chips(chip_sums):
                from_chips = _sequencer_chips(chip_sums, name + "_to_chips")

                def chip_sum(own):
                    total = _chip_sum(own, from_chips, name + "_chip_sum", tr=tile)
                    off = 0
                    for k, a in zip(keys, arrays):
                        self.reduced[k] = total[off:off + a.shape[1]]
                        off += a.shape[1]

                self.after(sum_marker, lambda: (chip_sums, chip_sum))

            return chip_sums, exchange_among_chips

        self.after(pair_marker, after_swap)


class _Weights:
    def __init__(self, inp):
        bf = lambda a: a.astype(BF16)
        local = {'w_in_t': bf(inp['w_in_ab'][0]).T, 'w_glu': bf(inp['w_glu_b'][0]), 'w_out': bf(inp['w_out_ab'][0]),
                 'pool_w': bf(inp['pool_w'][0]),
                 'small': _pack([inp[n] for n in SMALL_NAMES])}
        for l in range(2):
            local['wq%d' % l] = bf(inp['xa_wq'][l])
            local['wkv_t%d' % l] = bf(inp['xa_wkv'][l]).T
            local['wo%d' % l] = bf(inp['xa_wo'][l])
            local['up_t%d' % l] = bf(inp['ffn_w_up'][l]).T
            local['down%d' % l] = bf(inp['ffn_w_down'][l])
        self.local, self.full = local, {}

    def plan(self, keys):
        return _gather_comm([self.local[k] for k in keys])

    def gather_by_sequencer(self, keys, after, name, collective_id):
        arrays = [self.local[k] for k in keys]
        tie = (after.reshape(-1)[0] * 0.0).astype(arrays[0].dtype)
        arrays[0] = arrays[0] + tie
        self.land(keys, _sequencer_gather(arrays, name, collective_id))

    def land(self, keys, gathered):
        for k, g in zip(keys, gathered):
            if k == 'small':
                off = 0
                for n, size in zip(SMALL_NAMES, SMALL_SIZES):
                    self.full[n] = _merge_shards(g.reshape(N_DEV, -1)[:, off:off + size], *SMALL_SHARDED[n])
                    off += size
            elif k == 'pool_w':
                self.full[k] = jnp.swapaxes(g, 0, 1).reshape(len(POOL_WINDOWS), POOL_GROUP, POOL_GROUP)
            else:
                self.full[k] = g.reshape(N_DEV * g.shape[1], g.shape[2])


GATHER_FIRST = ['w_in_t', 'small']
GATHER_RIDES = []
GATHER_LAYER0 = ['w_glu', 'w_out', 'wq0', 'wkv_t0', 'wo0', 'down0', 'up_t0']
GATHER_LAYER1 = ['pool_w', 'wq1', 'wkv_t1', 'wo1', 'up_t1', 'down1']
GATHER_LAYER0_ID, GATHER_LAYER1_ID = 7, 8
GRAD_RIDES = [('g_down1', ['down1'], 'l1_ffn_act_bwd', 'l1_xa_norm_bwd'),
              ('g_up1', ['up_t1'], 'l1_xa_norm_bwd', 'l0_ffn_up_dx'),
              ('g_xa1', ['wq1', 'wkv_t1', 'wo1', 'pool_w'], 'l0_ffn_act_bwd', 'l0_xa_norm_bwd'),
              ('g_down0', ['down0'], 'l0_ffn_up_dx', 'l0_s5_scan_bwd'),
              ('g_l0', ['up_t0', 'wq0', 'wkv_t0', 'wo0'], 'l0_s5_cx_dx', 'l0_in_qkv_dx'),
              ('g_out', ['w_out', 'w_glu'], 'l0_s5_scan_bwd', 'l0_in_qkv_dx'),
              ('g_in', ['w_in_t'], 'l0_in_u_dx', 'adamw_pool_w')]


def _local_step(inp):
    f32_of = lambda n: inp[n].astype(F32)
    weights = _Weights(inp)
    riders = _Riders()
    riders.groups = list(GRAD_RIDES)
    full = weights.full
    weights.land(GATHER_FIRST, _comm_only(weights.plan(GATHER_FIRST), "gather_first"))
    weights.gather_by_sequencer(GATHER_LAYER0, full['w_in_t'], "gather_layer0", GATHER_LAYER0_ID)
    for host, keys in GATHER_RIDES:
        riders.add(host, weights.plan(keys), functools.partial(weights.land, keys))
    w_in_t = full['w_in_t']
    wts0 = dict(w_qkv_t=w_in_t[:3 * WIDTH_A], w_gate_t=w_in_t[3 * WIDTH_A:4 * WIDTH_A],
                w_ba_t=jnp.concatenate([w_in_t[4 * WIDTH_A:4 * WIDTH_A + 8], jnp.zeros((LANE - 8, D_MODEL), BF16)], 0),
                w_u_t=w_in_t[4 * WIDTH_A + 8:])
    lb_disc, disc_vjp = jax.vjp(_s5_discretise, f32_of('ssm_lambda_re')[0], f32_of('ssm_lambda_im')[0],
                                f32_of('ssm_b_re')[0], f32_of('ssm_b_im')[0], f32_of('ssm_log_dt')[0])
    b_in, c_out, a_row = _s5_matrices(*lb_disc, f32_of('ssm_c_re')[0], f32_of('ssm_c_im')[0])
    zeros4 = jnp.zeros((1, 4), F32)
    p0 = dict(conv_qkv=full['conv_qkv_a'][0], onorm_g=f32_of('onorm_g_a'),
              arow=jnp.concatenate([zeros4, f32_of('a_log_a'), jnp.zeros((1, LANE - 8), F32)], 1),
              brow=jnp.concatenate([zeros4, f32_of('dt_bias_a'), jnp.zeros((1, LANE - 8), F32)], 1),
              b_in=b_in.astype(BF16), c_out=c_out.astype(BF16), a_row=a_row,
              d_row=f32_of('ssm_d').reshape(1, SSM_WIDTH), b_glu=f32_of('b_glu_b'))

    x0 = inp['x'][0]
    mem_n = _rms_fwd(inp['mem'][0], inp['norm_mem_g'], BF16, "mem_norm")
    xn0 = _rms_fwd(x0, inp['norm_mix_g'][0], BF16, "l0_mix_norm")
    x1, sv_mix0 = _hybrid_fwd(xn0, x0, wts0, p0, weights, riders)
    x2, sv_xa0 = _xa_fwd(x1, inp['norm_xa_g'][0], mem_n, full['wq0'], full['wkv_t0'], full['wo0'], "l0_xa", riders)
    x3, sv_ffn0 = _ffn_fwd(x2, inp['norm_ffn_g'][0], full['up_t0'], full['ffn_conv'][0], full['down0'], "l0_ffn", riders)
    xn1 = _rms_fwd(x3, inp['norm_mix_g'][1], F32, "l1_mix_norm")
    x4 = _pool_fwd(xn1, full['pool_w'], full['pool_scale'], x3, "l1_pool")
    x5, sv_xa1 = _xa_fwd(x4, inp['norm_xa_g'][1], mem_n, full['wq1'], full['wkv_t1'], full['wo1'], "l1_xa", riders)
    x6, sv_ffn1 = _ffn_fwd(x5, inp['norm_ffn_g'][1], full['up_t1'], full['ffn_conv'][1], full['down1'], "l1_ffn", riders)
    loss_part, dx6, dg_final = _loss_head(x6, inp['norm_final_g'], inp['loss_target'][0], "loss_head")

    dx5, dconv1, dg_ffn1 = _ffn_bwd(dx6, x5, inp['norm_ffn_g'][1], full['up_t1'], full['ffn_conv'][1], full['down1'],
                                    sv_ffn1, "l1_ffn", 1, riders)
    dx4, dmem1, dg_xa1 = _xa_bwd(dx5, x4, inp['norm_xa_g'][1], mem_n, full['wq1'], full['wkv_t1'], full['wo1'],
                                 sv_xa1, "l1_xa", 1, riders)
    dxn1, dpool_w, dpool_scale = _pool_bwd(xn1, full['pool_w'], full['pool_scale'], dx4, "l1_pool_bwd")
    pool_pieces = jnp.swapaxes(dpool_w.astype(BF16).reshape(len(POOL_WINDOWS), N_DEV, -1, POOL_GROUP), 0, 1)
    riders.grad('pool_w', pool_pieces.reshape(N_DEV, -1, PACK_COLS))
    dx3, dg_mix1 = riders.run("l1_mix_norm_bwd", _rms_bwd, x3, inp['norm_mix_g'][1], dxn1, dx4)
    dx2, dconv0, dg_ffn0 = _ffn_bwd(dx3, x2, inp['norm_ffn_g'][0], full['up_t0'], full['ffn_conv'][0], full['down0'],
                                    sv_ffn0, "l0_ffn", 0, riders)
    dx1, dmem0, dg_xa0 = _xa_bwd(dx2, x1, inp['norm_xa_g'][0], mem_n, full['wq0'], full['wkv_t0'], full['wo0'],
                                 sv_xa0, "l0_xa", 0, riders)
    dxn0, g_mix0 = _hybrid_bwd(dx1, xn0, wts0, p0, sv_mix0, riders)
    grad_x, dg_mix0 = _rms_bwd(x0, inp['norm_mix_g'][0], dxn0, dx1, "l0_mix_norm_bwd")
    _, dg_mem = _rms_bwd(inp['mem'][0], inp['norm_mem_g'], dmem0 + dmem1, None, "mem_norm_bwd")
    assert not riders.groups and not riders.waiting and all(k.startswith("adamw_") for k in riders.deferred), (
        riders.groups, list(riders.waiting), list(riders.deferred))

    db_in, dc_out, da_row, dd = g_mix0['s5']
    dlb_re, dlb_im, dbb_re, dbb_im, dc_re, dc_im = _s5_matrix_grads(db_in, dc_out, da_row)
    dlam_re, dlam_im, dbr, dbi, dlog_dt = disc_vjp((dlb_re, dlb_im, dbb_re, dbb_im))

    rep_grads = {
        'norm_mix_g': jnp.concatenate([dg_mix0, dg_mix1], 0), 'norm_xa_g': jnp.concatenate([dg_xa0, dg_xa1], 0),
        'norm_ffn_g': jnp.concatenate([dg_ffn0, dg_ffn1], 0), 'norm_mem_g': dg_mem.reshape(-1),
        'norm_final_g': dg_final.reshape(-1), 'a_log_a': g_mix0['a_log_a'], 'dt_bias_a': g_mix0['dt_bias_a'],
        'onorm_g_a': g_mix0['onorm_g_a'], 'ssm_lambda_re': dlam_re[None], 'ssm_lambda_im': dlam_im[None],
        'ssm_b_re': dbr[None], 'ssm_b_im': dbi[None], 'ssm_c_re': dc_re[None], 'ssm_c_im': dc_im[None],
        'ssm_d': dd.reshape(1, N_GROUPS, SSM_GROUP), 'ssm_log_dt': dlog_dt[None], 'b_glu_b': g_mix0['b_glu_b']}
    small_grads = {'conv_qkv_a': g_mix0['conv_qkv_a'][None], 'pool_scale': dpool_scale,
                   'ffn_conv': jnp.stack([dconv0, dconv1])}
    return loss_part, grad_x, riders, rep_grads, small_grads


ADAMW_ORDER = ['ffn_w_up', 'ffn_w_down', 'xa_wkv', 'xa_wq', 'xa_wo', 'w_out_ab', 'w_glu_b', 'pool_w', 'w_in_ab']


def _update(inp, loss_part, grad_x, riders, rep_grads, small_grads):
    dev = _device_index()
    misc_local = _pack([rep_grads[n] for n in REP_NAMES] + [small_grads[n] for n in SMALL_NAMES] + [loss_part])
    (misc_all,) = _sequencer_gather([misc_local], "gather_small_grads", GATHER_LAYER0_ID)
    piece = lambda key: riders.reduced[key]
    both = lambda name: jnp.stack([piece(name + '0'), piece(name + '1')])
    swap = lambda a: jnp.swapaxes(a, -1, -2)
    reduced = {'w_in_ab': lambda: piece('w_in_t')[:W_IN_PIECE][None],
               'w_glu_b': lambda: piece('w_glu').reshape(inp['w_glu_b'].shape),
               'w_out_ab': lambda: piece('w_out')[None], 'pool_w': lambda: piece('pool_w').reshape(inp['pool_w'].shape),
               'xa_wq': lambda: both('wq'), 'xa_wkv': lambda: both('wkv_t'), 'xa_wo': lambda: both('wo'),
               'ffn_w_up': lambda: both('up_t'), 'ffn_w_down': lambda: both('down')}
    transposed = ('w_in_ab', 'xa_wkv', 'ffn_w_up')
    grads, upd = {}, {}
    assert sorted(ADAMW_ORDER) == sorted(BIG_NAMES)
    for n in ADAMW_ORDER:
        fix = swap if n in transposed else (lambda a: a)
        g = reduced[n]()
        out = riders.run("adamw_" + n, _adamw, fix(inp[n]), g, fix(inp['m_' + n]), fix(inp['v_' + n]))
        upd[n], grads[n] = tuple(fix(o) for o in out), fix(g)
    assert not riders.waiting and not riders.deferred, (list(riders.waiting), list(riders.deferred))
    misc_sum = _sum_leading(misc_all, "small_grads_sum")
    misc = _unpack(misc_sum, [inp[n].shape for n in REP_NAMES] + [SMALL_SHARDED[n][0] for n in SMALL_NAMES] + [()])
    loss = misc.pop()
    for n, g in zip(REP_NAMES, misc):
        grads[n] = g
    for n, g in zip(SMALL_NAMES, misc[len(REP_NAMES):]):
        grads[n] = lax.dynamic_index_in_dim(_split_shards(g, SMALL_SHARDED[n][1]), dev, 0, keepdims=False
                                            ).reshape(inp[n].shape)
    tiny_names = REP_NAMES + SMALL_NAMES
    rep_total = sum(int(np.prod(inp[n].shape)) for n in REP_NAMES)
    packs = [_pack([inp[prefix + n] for n in tiny_names]) for prefix in ('', 'm_', 'v_')]
    g_pack = _pack([misc_sum.reshape(-1)[:rep_total]] + [grads[n] for n in SMALL_NAMES])
    tiny_out = [_unpack(o, [inp[n].shape for n in tiny_names])
                for o in _adamw(packs[0], g_pack, packs[1], packs[2], "adamw_small")]
    for i, n in enumerate(tiny_names):
        upd[n] = tuple(o[i] for o in tiny_out)

    outs = [loss, grad_x[None]]
    outs += [grads[n] for n in WEIGHT_NAMES]
    for i in range(3):
        outs += [upd[n][i] for n in WEIGHT_NAMES]
    return tuple(outs)


def _step(inp):
    loss_part, grad_x, riders, rep_grads, small_grads = _local_step(inp)
    return _update(inp, loss_part, grad_x, riders, rep_grads, small_grads)


INPUT_NAMES = (['x', 'mem'] + WEIGHT_NAMES + ['loss_target'] + ['m_' + n for n in WEIGHT_NAMES]
               + ['v_' + n for n in WEIGHT_NAMES])


def kernel(x, mem, norm_mix_g, norm_xa_g, norm_ffn_g, norm_mem_g, norm_final_g, w_in_ab, conv_qkv_a, a_log_a, dt_bias_a, onorm_g_a, ssm_lambda_re, ssm_lambda_im, ssm_b_re, ssm_b_im, ssm_c_re, ssm_c_im, ssm_d, ssm_log_dt, w_glu_b, b_glu_b, w_out_ab, pool_w, pool_scale, xa_wq, xa_wkv, xa_wo, ffn_w_up, ffn_conv, ffn_w_down, loss_target, m_norm_mix_g, m_norm_xa_g, m_norm_ffn_g, m_norm_mem_g, m_norm_final_g, m_w_in_ab, m_conv_qkv_a, m_a_log_a, m_dt_bias_a, m_onorm_g_a, m_ssm_lambda_re, m_ssm_lambda_im, m_ssm_b_re, m_ssm_b_im, m_ssm_c_re, m_ssm_c_im, m_ssm_d, m_ssm_log_dt, m_w_glu_b, m_b_glu_b, m_w_out_ab, m_pool_w, m_pool_scale, m_xa_wq, m_xa_wkv, m_xa_wo, m_ffn_w_up, m_ffn_conv, m_ffn_w_down, v_norm_mix_g, v_norm_xa_g, v_norm_ffn_g, v_norm_mem_g, v_norm_final_g, v_w_in_ab, v_conv_qkv_a, v_a_log_a, v_dt_bias_a, v_onorm_g_a, v_ssm_lambda_re, v_ssm_lambda_im, v_ssm_b_re, v_ssm_b_im, v_ssm_c_re, v_ssm_c_im, v_ssm_d, v_ssm_log_dt, v_w_glu_b, v_b_glu_b, v_w_out_ab, v_pool_w, v_pool_scale, v_xa_wq, v_xa_wkv, v_xa_wo, v_ffn_w_up, v_ffn_conv, v_ffn_w_down):
    args = (x, mem, norm_mix_g, norm_xa_g, norm_ffn_g, norm_mem_g, norm_final_g, w_in_ab, conv_qkv_a, a_log_a, dt_bias_a, onorm_g_a, ssm_lambda_re, ssm_lambda_im, ssm_b_re, ssm_b_im, ssm_c_re, ssm_c_im, ssm_d, ssm_log_dt, w_glu_b, b_glu_b, w_out_ab, pool_w, pool_scale, xa_wq, xa_wkv, xa_wo, ffn_w_up, ffn_conv, ffn_w_down, loss_target, m_norm_mix_g, m_norm_xa_g, m_norm_ffn_g, m_norm_mem_g, m_norm_final_g, m_w_in_ab, m_conv_qkv_a, m_a_log_a, m_dt_bias_a, m_onorm_g_a, m_ssm_lambda_re, m_ssm_lambda_im, m_ssm_b_re, m_ssm_b_im, m_ssm_c_re, m_ssm_c_im, m_ssm_d, m_ssm_log_dt, m_w_glu_b, m_b_glu_b, m_w_out_ab, m_pool_w, m_pool_scale, m_xa_wq, m_xa_wkv, m_xa_wo, m_ffn_w_up, m_ffn_conv, m_ffn_w_down, v_norm_mix_g, v_norm_xa_g, v_norm_ffn_g, v_norm_mem_g, v_norm_final_g, v_w_in_ab, v_conv_qkv_a, v_a_log_a, v_dt_bias_a, v_onorm_g_a, v_ssm_lambda_re, v_ssm_lambda_im, v_ssm_b_re, v_ssm_b_im, v_ssm_c_re, v_ssm_c_im, v_ssm_d, v_ssm_log_dt, v_w_glu_b, v_b_glu_b, v_w_out_ab, v_pool_w, v_pool_scale, v_xa_wq, v_xa_wkv, v_xa_wo, v_ffn_w_up, v_ffn_conv, v_ffn_w_down)
    return _step(dict(zip(INPUT_NAMES, args)))
```

```python
import functools
import math

import numpy as np
import jax
import jax.numpy as jnp
from jax import lax
from jax.experimental import pallas as pl
from jax.experimental.pallas import tpu as pltpu
from jax.experimental.pallas import tpu_sc as plsc

F32, BF16 = jnp.float32, jnp.bfloat16
HIGH, HIGHEST = lax.Precision.HIGH, lax.Precision.HIGHEST
MESH = pl.DeviceIdType.MESH

N_DEV = 8
SEQ, D_MODEL, MEM_LEN = 2048, 1024, 256
WIDTH_A, N_HEADS_A, HEAD_A, CONV_A = 512, 4, 128, 4
GDR_CHUNK = 128
GDR_HEADS_PER_STEP = 4
SSM_WIDTH, SSM_GROUP, N_GROUPS, SSM_STATE = 512, 16, 32, 64
SSM_CH = N_GROUPS * SSM_STATE
SCAN_CB = 512
POOL_WINDOWS = (2, 4, 8, 16)
POOL_GROUP = 256
N_HEADS_X, HEAD_X = 4, 256
D_FF, CONV_FFN = 2816, 3
RMS_EPS = 1e-6
ADAM_LR, ADAM_B1, ADAM_B2, ADAM_EPS, ADAM_WD, ADAM_STEP = 0.001, 0.9, 0.999, 1e-08, 0.01, 10
LANE = 128
PACK_COLS = 1024
VMEM_LIMIT_BYTES = 56 * 1024 * 1024


def _params(sem=None):
    return pltpu.CompilerParams(dimension_semantics=sem, vmem_limit_bytes=VMEM_LIMIT_BYTES)


class Comm:
    def __init__(self, inputs, out_shapes, sems, start, end, mid=None):
        self.inputs, self.out_shapes, self.sems = list(inputs), list(out_shapes), list(sems)
        self.start, self.mid, self.end = start, mid, end


def _merge_comms(comms):
    comms = [c for c in comms if c is not None]
    if not comms:
        return None, []
    bounds, ni, no, ns = [], 0, 0, 0
    for c in comms:
        bounds.append((ni, no, ns))
        ni, no, ns = ni + len(c.inputs), no + len(c.out_shapes), ns + len(c.sems)

    def phase(which):
        def run(ins, outs, sems):
            for c, (i0, o0, s0) in zip(comms, bounds):
                fn = getattr(c, which)
                if fn is not None:
                    fn(ins[i0:i0 + len(c.inputs)], outs[o0:o0 + len(c.out_shapes)], sems[s0:s0 + len(c.sems)])
        return run

    merged = Comm([a for c in comms for a in c.inputs], [s for c in comms for s in c.out_shapes],
                  [s for c in comms for s in c.sems], phase("start"), phase("end"), phase("mid"))
    return merged, [(o0, o0 + len(c.out_shapes)) for c, (_, o0, _) in zip(comms, bounds)]


def _call(body, *, name, grid, in_specs, out_specs, out_shape, args, scratch_shapes=(), sem=None, comm=None):
    single = not isinstance(out_shape, (list, tuple))
    out_specs_l = [out_specs] if single else list(out_specs)
    out_shape_l = [out_shape] if single else list(out_shape)
    scratch_shapes = list(scratch_shapes)
    merged, spans = _merge_comms(comm if isinstance(comm, (list, tuple)) else [comm])
    if merged is None:
        outs = pl.pallas_call(body, name=name, grid=grid, in_specs=list(in_specs), out_specs=out_specs_l,
                              out_shape=out_shape_l, scratch_shapes=scratch_shapes, compiler_params=_params(sem))(*args)
        outs = outs[0] if single else outs
        return outs if comm is None else (outs, [])
    n_in, n_out, n_scr = len(in_specs), len(out_specs_l), len(scratch_shapes)
    ci, co = len(merged.inputs), len(merged.out_shapes)
    total = int(np.prod(grid))

    def wrapped(*refs):
        ins, cins = refs[:n_in], refs[n_in:n_in + ci]
        outs, couts = refs[n_in + ci:n_in + ci + n_out], refs[n_in + ci + n_out:n_in + ci + n_out + co]
        scr, csems = refs[n_in + ci + n_out + co:n_in + ci + n_out + co + n_scr], refs[n_in + ci + n_out + co + n_scr:]
        lin = pl.program_id(0)
        for d in range(1, len(grid)):
            lin = lin * grid[d] + pl.program_id(d)
        pl.when(lin == 0)(lambda: merged.start(cins, couts, csems))
        body(*ins, *outs, *scr)
        mid_step = min((3 * total) // 4, total - 1)
        pl.when(lin == mid_step)(lambda: merged.mid(cins, couts, csems))
        pl.when(lin == total - 1)(lambda: merged.end(cins, couts, csems))

    any_spec = pl.BlockSpec(memory_space=pl.ANY)
    res = pl.pallas_call(
        wrapped, name=name, grid=grid, in_specs=list(in_specs) + [any_spec] * ci,
        out_specs=out_specs_l + [any_spec] * co, out_shape=out_shape_l + merged.out_shapes,
        scratch_shapes=scratch_shapes + merged.sems,
        compiler_params=_params(("arbitrary",) * len(grid)))(*args, *merged.inputs)
    outs, couts = res[:n_out], res[n_out:]
    return (outs[0] if single else list(outs)), [list(couts[a:b]) for a, b in spans]


def _comm_only(comm, name):
    def body():
        pass

    _, couts = _call(body, name=name, grid=(1,), in_specs=[], out_specs=[], out_shape=[], args=[], comm=comm)
    return couts[0]


def _tile(dim, pref):
    best = None
    for t in range(LANE, min(dim, pref) + 1, LANE):
        if dim % t == 0:
            best = t
    return best if best is not None else dim


MM_VMEM_BUDGET = 40 * 1024 * 1024


def _mm_tiles(m, n, k, a_bytes, b_bytes, o_bytes, r_bytes):
    for tk in (k, _tile(k, 2048), _tile(k, 1024), _tile(k, 512)):
        for tm, tn in ((1024, 1536), (1024, 1024), (1024, 512), (512, 512), (256, 512), (256, 256)):
            tm, tn = _tile(m, tm), _tile(n, tn)
            acc = 0 if tk == k else tm * tn * 4
            need = 2 * (tm * tk * a_bytes + tk * tn * b_bytes + tm * tn * (o_bytes + r_bytes)) + acc
            if need <= MM_VMEM_BUDGET:
                return tm, tn, tk
    raise ValueError("no matmul tiling fits VMEM")


def _mm(a, b, mode, name, out_dtype=F32, res=None, comm=None):
    if mode == "nn":
        (m, k), n = a.shape, b.shape[1]
    elif mode == "nt":
        (m, k), n = a.shape, b.shape[0]
    else:
        (k, m), n = a.shape, b.shape[1]
    tm, tn, tk = _mm_tiles(m, n, k, a.dtype.itemsize, b.dtype.itemsize, jnp.dtype(out_dtype).itemsize,
                           0 if res is None else res.dtype.itemsize)
    nk = k // tk
    dims = {"nn": ((1,), (0,)), "nt": ((1,), (1,)), "tn": ((0,), (0,))}[mode]

    def body(*refs):
        if res is None:
            a_ref, b_ref, o_ref = refs[:3]
            r_ref = None
        else:
            a_ref, b_ref, r_ref, o_ref = refs[:4]
        part = lax.dot_general(a_ref[...].astype(BF16), b_ref[...].astype(BF16), (dims, ((), ())),
                               preferred_element_type=F32)

        def finish(out):
            if r_ref is not None:
                out = out + r_ref[...].astype(F32)
            o_ref[...] = out.astype(out_dtype)

        if nk == 1:
            finish(part)
            return
        acc = refs[-1]
        kk = pl.program_id(2)

        @pl.when(kk == 0)
        def _():
            acc[...] = part

        @pl.when(kk > 0)
        def _():
            acc[...] += part

        @pl.when(kk == nk - 1)
        def _():
            finish(acc[...])

    a_spec = (pl.BlockSpec((tk, tm), lambda i, j, q: (q, i)) if mode == "tn"
              else pl.BlockSpec((tm, tk), lambda i, j, q: (i, q)))
    b_spec = (pl.BlockSpec((tn, tk), lambda i, j, q: (j, q)) if mode == "nt"
              else pl.BlockSpec((tk, tn), lambda i, j, q: (q, j)))
    o_spec = pl.BlockSpec((tm, tn), lambda i, j, q: (i, j))
    in_specs, args = [a_spec, b_spec], [a, b]
    if res is not None:
        in_specs.append(o_spec)
        args.append(res)
    return _call(body, name=name, grid=(m // tm, n // tn, nk), in_specs=in_specs, out_specs=o_spec,
                 out_shape=jax.ShapeDtypeStruct((m, n), out_dtype),
                 scratch_shapes=[] if nk == 1 else [pltpu.VMEM((tm, tn), F32)],
                 sem=("parallel", "parallel", "arbitrary"), args=args, comm=comm)


def _mm_bd(a, b, mode, name, out_dtype=F32, res=None, comm=None, tm=1024):
    if mode == "tn":
        k = a.shape[0]
        nb = min(a.shape[1], b.shape[1]) // LANE
        ma, n = a.shape[1] // nb, b.shape[1] // nb

        def body(a_ref, b_ref, o_ref):
            o_ref[0] = lax.dot_general(a_ref[...].astype(BF16), b_ref[...].astype(BF16), (((0,), (0,)), ((), ())),
                                       preferred_element_type=F32).astype(out_dtype)

        return _call(body, name=name, grid=(nb,),
                     in_specs=[pl.BlockSpec((k, ma), lambda j: (0, j)), pl.BlockSpec((k, n), lambda j: (0, j))],
                     out_specs=pl.BlockSpec((1, ma, n), lambda j: (j, 0, 0)),
                     out_shape=jax.ShapeDtypeStruct((nb, ma, n), out_dtype), sem=("parallel",), args=(a, b), comm=comm)
    m = a.shape[0]
    nb = b.shape[0]
    ka = a.shape[1] // nb
    n = b.shape[2] if mode == "nn" else b.shape[1]
    tm = _tile(m, tm)
    dims = ((1,), (0,)) if mode == "nn" else ((1,), (1,))

    def body(*refs):
        if res is None:
            a_ref, b_ref, o_ref = refs
            r_ref = None
        else:
            a_ref, b_ref, r_ref, o_ref = refs
        out = lax.dot_general(a_ref[...].astype(BF16), b_ref[0].astype(BF16), (dims, ((), ())),
                              preferred_element_type=F32)
        if r_ref is not None:
            out = out + r_ref[...].astype(F32)
        o_ref[...] = out.astype(out_dtype)

    o_spec = pl.BlockSpec((tm, n), lambda i, j: (i, j))
    in_specs = [pl.BlockSpec((tm, ka), lambda i, j: (i, j)), pl.BlockSpec((1,) + b.shape[1:], lambda i, j: (j, 0, 0))]
    args = [a, b]
    if res is not None:
        in_specs.append(o_spec)
        args.append(res)
    return _call(body, name=name, grid=(m // tm, nb), in_specs=in_specs, out_specs=o_spec,
                 out_shape=jax.ShapeDtypeStruct((m, nb * n), out_dtype), sem=("parallel", "parallel"),
                 args=args, comm=comm)


def _rms_fwd(x, g, out_dtype, name, tr=256):
    rows, d = x.shape

    def body(x_ref, g_ref, o_ref):
        xv = x_ref[...]
        r = lax.rsqrt(jnp.mean(xv * xv, axis=-1, keepdims=True) + RMS_EPS)
        o_ref[...] = (xv * r * g_ref[...]).astype(out_dtype)

    return pl.pallas_call(
        body, name=name, grid=(rows // tr,),
        in_specs=[pl.BlockSpec((tr, d), lambda i: (i, 0)), pl.BlockSpec((1, d), lambda i: (0, 0))],
        out_specs=pl.BlockSpec((tr, d), lambda i: (i, 0)), out_shape=jax.ShapeDtypeStruct((rows, d), out_dtype),
        compiler_params=_params(("parallel",)))(x, g.reshape(1, d))


def _rms_bwd(x, g, dy, dres, name, tr=256, comm=None):
    rows, d = x.shape

    def body(*refs):
        if dres is None:
            x_ref, g_ref, dy_ref, dx_ref, dg_ref = refs
            r_ref = None
        else:
            x_ref, g_ref, dy_ref, r_ref, dx_ref, dg_ref = refs

        @pl.when(pl.program_id(0) == 0)
        def _():
            dg_ref[...] = jnp.zeros_like(dg_ref)

        xv, dyv = x_ref[...], dy_ref[...].astype(F32)
        r = lax.rsqrt(jnp.mean(xv * xv, axis=-1, keepdims=True) + RMS_EPS)
        xh = xv * r
        dyg = dyv * g_ref[...]
        dx = r * (dyg - xh * jnp.mean(dyg * xh, axis=-1, keepdims=True))
        if r_ref is not None:
            dx = dx + r_ref[...]
        dx_ref[...] = dx
        dg_ref[...] += jnp.sum(dyv * xh, axis=0, keepdims=True)

    blk = pl.BlockSpec((tr, d), lambda i: (i, 0))
    vec = pl.BlockSpec((1, d), lambda i: (0, 0))
    in_specs, args = [blk, vec, blk], [x, g.reshape(1, d), dy]
    if dres is not None:
        in_specs.append(blk)
        args.append(dres)
    return _call(
        body, name=name, grid=(rows // tr,), in_specs=in_specs, out_specs=[blk, vec],
        out_shape=[jax.ShapeDtypeStruct((rows, d), F32), jax.ShapeDtypeStruct((1, d), F32)],
        sem=("arbitrary",), args=args, comm=comm)


def _loss_head(x, g, target, name, tr=256):
    rows, d = x.shape

    def body(x_ref, g_ref, t_ref, loss_ref, dx_ref, dg_ref):
        @pl.when(pl.program_id(0) == 0)
        def _():
            dg_ref[...] = jnp.zeros_like(dg_ref)
            loss_ref[...] = jnp.zeros_like(loss_ref)

        xv = x_ref[...]
        r = lax.rsqrt(jnp.mean(xv * xv, axis=-1, keepdims=True) + RMS_EPS)
        xh = xv * r
        err = xh * g_ref[...] - t_ref[...]
        loss_ref[...] += 0.5 * jnp.sum(jnp.mean(err * err, axis=-1, keepdims=True), keepdims=True)
        dyv = err * (1.0 / d)
        dyg = dyv * g_ref[...]
        dx_ref[...] = r * (dyg - xh * jnp.mean(dyg * xh, axis=-1, keepdims=True))
        dg_ref[...] += jnp.sum(dyv * xh, axis=0, keepdims=True)

    blk = pl.BlockSpec((tr, d), lambda i: (i, 0))
    vec = pl.BlockSpec((1, d), lambda i: (0, 0))
    return pl.pallas_call(
        body, name=name, grid=(rows // tr,), in_specs=[blk, vec, blk],
        out_specs=[pl.BlockSpec((1, 1), lambda i: (0, 0)), blk, vec],
        out_shape=[jax.ShapeDtypeStruct((1, 1), F32), jax.ShapeDtypeStruct((rows, d), F32),
                   jax.ShapeDtypeStruct((1, d), F32)],
        compiler_params=_params(("arbitrary",)))(x, g.reshape(1, d), target)


def _shift_down(x, s):
    rows = lax.broadcasted_iota(jnp.int32, x.shape, 0)
    return jnp.where(rows >= s, pltpu.roll(x, s, 0), 0.0)


def _shift_up(x, s):
    n = x.shape[0]
    rows = lax.broadcasted_iota(jnp.int32, x.shape, 0)
    return jnp.where(rows < n - s, pltpu.roll(x, n - s, 0), 0.0)


def _sigmoid(x):
    return 1.0 / (1.0 + jnp.exp(-x))


def _silu_and_grad(x):
    s = _sigmoid(x)
    return x * s, s * (1.0 + x * (1.0 - s))


_GELU_C0, _GELU_C1 = math.sqrt(2.0 / math.pi), 0.044715


def _gelu_and_grad(x):
    th = jnp.tanh(_GELU_C0 * (x + _GELU_C1 * x * x * x))
    y = 0.5 * x * (1.0 + th)
    dy = 0.5 * (1.0 + th) + 0.5 * x * (1.0 - th * th) * _GELU_C0 * (1.0 + 3.0 * _GELU_C1 * x * x)
    return y, dy


def _ffn_act_fwd(h, w, name, tc=256, comm=None):
    t = h.shape[0]
    nb = D_FF // tc

    def body(hg_ref, hv_ref, wg_ref, wv_ref, a_ref):
        def conv(x, wr):
            return wr[2:3, :] * x + wr[1:2, :] * _shift_down(x, 1) + wr[0:1, :] * _shift_down(x, 2)

        cg = conv(hg_ref[...], wg_ref[...])
        cv = conv(hv_ref[...], wv_ref[...])
        a_ref[...] = (cg * _sigmoid(cg) * cv).astype(BF16)

    return _call(
        body, name=name, grid=(nb,),
        in_specs=[pl.BlockSpec((t, tc), lambda j: (0, j)), pl.BlockSpec((t, tc), lambda j: (0, j + nb)),
                  pl.BlockSpec((CONV_FFN, tc), lambda j: (0, j)), pl.BlockSpec((CONV_FFN, tc), lambda j: (0, j + nb))],
        out_specs=pl.BlockSpec((t, tc), lambda j: (0, j)), out_shape=jax.ShapeDtypeStruct((t, D_FF), BF16),
        sem=("parallel",), args=(h, h, w, w), comm=comm)


def _ffn_act_bwd(h, w, da, name, tc=256, comm=None):
    t = h.shape[0]
    nb = D_FF // tc

    def body(hg_ref, hv_ref, wg_ref, wv_ref, da_ref, dhg_ref, dhv_ref, dwg_ref, dwv_ref):
        hg, hv, wg, wv = hg_ref[...], hv_ref[...], wg_ref[...], wv_ref[...]
        hg1, hg2, hv1, hv2 = _shift_down(hg, 1), _shift_down(hg, 2), _shift_down(hv, 1), _shift_down(hv, 2)
        cg = wg[2:3, :] * hg + wg[1:2, :] * hg1 + wg[0:1, :] * hg2
        cv = wv[2:3, :] * hv + wv[1:2, :] * hv1 + wv[0:1, :] * hv2
        sg, dsg = _silu_and_grad(cg)
        dav = da_ref[...].astype(F32)
        dcv = dav * sg
        dcg = dav * cv * dsg

        def conv_t(dc, wr):
            return wr[2:3, :] * dc + wr[1:2, :] * _shift_up(dc, 1) + wr[0:1, :] * _shift_up(dc, 2)

        dhg_ref[...] = conv_t(dcg, wg).astype(BF16)
        dhv_ref[...] = conv_t(dcv, wv).astype(BF16)
        dwg_ref[0:1, :] = jnp.sum(dcg * hg2, axis=0, keepdims=True)
        dwg_ref[1:2, :] = jnp.sum(dcg * hg1, axis=0, keepdims=True)
        dwg_ref[2:3, :] = jnp.sum(dcg * hg, axis=0, keepdims=True)
        dwv_ref[0:1, :] = jnp.sum(dcv * hv2, axis=0, keepdims=True)
        dwv_ref[1:2, :] = jnp.sum(dcv * hv1, axis=0, keepdims=True)
        dwv_ref[2:3, :] = jnp.sum(dcv * hv, axis=0, keepdims=True)

    big = lambda off: pl.BlockSpec((t, tc), lambda j: (0, j + off))
    small = lambda off: pl.BlockSpec((CONV_FFN, tc), lambda j: (0, j + off))
    res = _call(
        body, name=name, grid=(nb,),
        in_specs=[big(0), big(nb), small(0), small(nb), big(0)],
        out_specs=[big(0), big(0), small(0), small(0)],
        out_shape=[jax.ShapeDtypeStruct((t, D_FF), BF16), jax.ShapeDtypeStruct((t, D_FF), BF16),
                   jax.ShapeDtypeStruct((CONV_FFN, D_FF), F32), jax.ShapeDtypeStruct((CONV_FFN, D_FF), F32)],
        sem=("parallel",), args=(h, h, w, w, da), comm=comm)
    (dhg, dhv, dwg, dwv), couts = res if comm is not None else (res, None)
    out = (jnp.concatenate([dhg, dhv], axis=1), jnp.concatenate([dwg, dwv], axis=1))
    return out if comm is None else (out, couts)


def _attn_probs(q, k):
    s = lax.dot_general(q.astype(BF16), k.astype(BF16), (((1,), (1,)), ((), ())),
                        preferred_element_type=F32) * (HEAD_X ** -0.5)
    s = s - jnp.max(s, axis=-1, keepdims=True)
    p = jnp.exp(s)
    return p / jnp.sum(p, axis=-1, keepdims=True)


def _attn_fwd(q, kv, name, tq=512, comm=None):
    t = q.shape[0]

    def body(q_ref, k_ref, v_ref, o_ref):
        p = _attn_probs(q_ref[...], k_ref[...])
        o_ref[...] = jnp.dot(p.astype(BF16), v_ref[...].astype(BF16), preferred_element_type=F32).astype(BF16)

    return _call(
        body, name=name, grid=(N_HEADS_X, t // tq),
        in_specs=[pl.BlockSpec((tq, HEAD_X), lambda h, i: (i, h)),
                  pl.BlockSpec((MEM_LEN, HEAD_X), lambda h, i: (0, h)),
                  pl.BlockSpec((MEM_LEN, HEAD_X), lambda h, i: (0, h + N_HEADS_X))],
        out_specs=pl.BlockSpec((tq, HEAD_X), lambda h, i: (i, h)),
        out_shape=jax.ShapeDtypeStruct((t, N_HEADS_X * HEAD_X), BF16),
        sem=("parallel", "parallel"), args=(q, kv, kv), comm=comm)


def _attn_bwd(q, kv, do, name, tq=512):
    t = q.shape[0]

    def body(q_ref, k_ref, v_ref, do_ref, dq_ref, dk_ref, dv_ref):
        @pl.when(pl.program_id(1) == 0)
        def _():
            dk_ref[...] = jnp.zeros_like(dk_ref)
            dv_ref[...] = jnp.zeros_like(dv_ref)

        qb, kb, vb, dob = (r[...].astype(BF16) for r in (q_ref, k_ref, v_ref, do_ref))
        p = _attn_probs(qb, kb)
        dp = lax.dot_general(dob, vb, (((1,), (1,)), ((), ())), preferred_element_type=F32)
        ds = p * (dp - jnp.sum(dp * p, axis=-1, keepdims=True)) * (HEAD_X ** -0.5)
        dsb = ds.astype(BF16)
        dq_ref[...] = jnp.dot(dsb, kb, preferred_element_type=F32).astype(BF16)
        dk_ref[...] += lax.dot_general(dsb, qb, (((0,), (0,)), ((), ())), preferred_element_type=F32)
        dv_ref[...] += lax.dot_general(p.astype(BF16), dob, (((0,), (0,)), ((), ())), preferred_element_type=F32)

    qs = pl.BlockSpec((tq, HEAD_X), lambda h, i: (i, h))
    ms = pl.BlockSpec((MEM_LEN, HEAD_X), lambda h, i: (0, h))
    return pl.pallas_call(
        body, name=name, grid=(N_HEADS_X, t // tq),
        in_specs=[qs, ms, pl.BlockSpec((MEM_LEN, HEAD_X), lambda h, i: (0, h + N_HEADS_X)), qs],
        out_specs=[qs, ms, ms],
        out_shape=[jax.ShapeDtypeStruct((t, D_MODEL), BF16), jax.ShapeDtypeStruct((MEM_LEN, D_MODEL), F32),
                   jax.ShapeDtypeStruct((MEM_LEN, D_MODEL), F32)],
        compiler_params=_params(("parallel", "arbitrary")))(q, kv, kv, do)


def _pool_counts(t, win):
    pos = lax.broadcasted_iota(jnp.int32, (t, 1), 0).astype(F32) + 1.0
    return 1.0 / jnp.minimum(pos, float(win))


def _pool_delta(xv, win):
    s, step = xv, 1
    while step < win:
        s = s + _shift_down(s, step)
        step *= 2
    return s * _pool_counts(xv.shape[0], win) - xv


def _pool_delta_t(dv, win):
    s, step = dv * _pool_counts(dv.shape[0], win), 1
    while step < win:
        s = s + _shift_up(s, step)
        step *= 2
    return s - dv


def _pool_fwd(xn, w, scale, res, name):
    t = xn.shape[0]

    def make_branch(win, xn_ref, w_ref, s_ref, r_ref, o_ref):
        def branch():
            dl = _pool_delta(xn_ref[...], win)
            y = jnp.dot(dl.astype(BF16), w_ref[0], preferred_element_type=F32)
            o_ref[...] = r_ref[...] + y * s_ref[...]
        return branch

    def body(xn_ref, w_ref, s_ref, r_ref, o_ref):
        for gi, win in enumerate(POOL_WINDOWS):
            pl.when(pl.program_id(0) == gi)(make_branch(win, xn_ref, w_ref, s_ref, r_ref, o_ref))

    blk = pl.BlockSpec((t, POOL_GROUP), lambda g: (0, g))
    return pl.pallas_call(
        body, name=name, grid=(len(POOL_WINDOWS),),
        in_specs=[blk, pl.BlockSpec((1, POOL_GROUP, POOL_GROUP), lambda g: (g, 0, 0)),
                  pl.BlockSpec((1, POOL_GROUP), lambda g: (0, g)), blk],
        out_specs=blk, out_shape=jax.ShapeDtypeStruct((t, D_MODEL), F32),
        compiler_params=_params(("parallel",)))(xn, w, scale, res)


def _pool_bwd(xn, w, scale, dmix, name):
    t = xn.shape[0]

    def make_branch(win, xn_ref, w_ref, s_ref, d_ref, dxn_ref, dw_ref, ds_ref):
        def branch():
            dl = _pool_delta(xn_ref[...], win).astype(BF16)
            wv = w_ref[0]
            dm = d_ref[...]
            y = jnp.dot(dl, wv, preferred_element_type=F32)
            ds_ref[...] = jnp.sum(dm * y, axis=0, keepdims=True)
            dy = (dm * s_ref[...]).astype(BF16)
            dw_ref[0] = lax.dot_general(dl, dy, (((0,), (0,)), ((), ())), preferred_element_type=F32)
            ddl = lax.dot_general(dy, wv, (((1,), (1,)), ((), ())), preferred_element_type=F32)
            dxn_ref[...] = _pool_delta_t(ddl, win)
        return branch

    def body(*refs):
        for gi, win in enumerate(POOL_WINDOWS):
            pl.when(pl.program_id(0) == gi)(make_branch(win, *refs))

    blk = pl.BlockSpec((t, POOL_GROUP), lambda g: (0, g))
    wspec = pl.BlockSpec((1, POOL_GROUP, POOL_GROUP), lambda g: (g, 0, 0))
    vec = pl.BlockSpec((1, POOL_GROUP), lambda g: (0, g))
    return pl.pallas_call(
        body, name=name, grid=(len(POOL_WINDOWS),), in_specs=[blk, wspec, vec, blk], out_specs=[blk, wspec, vec],
        out_shape=[jax.ShapeDtypeStruct((t, D_MODEL), F32),
                   jax.ShapeDtypeStruct((len(POOL_WINDOWS), POOL_GROUP, POOL_GROUP), F32),
                   jax.ShapeDtypeStruct((1, D_MODEL), F32)],
        compiler_params=_params(("parallel",)))(xn, w, scale, dmix)


def _qkv_conv(h, wr):
    return (wr[3:4, :] * h + wr[2:3, :] * _shift_down(h, 1) + wr[1:2, :] * _shift_down(h, 2)
            + wr[0:1, :] * _shift_down(h, 3))


def _qkv_pre_fwd(h, w, col0, ncols, normalize, scale, name):
    t = h.shape[0]

    def body(h_ref, w_ref, o_ref):
        c = _qkv_conv(h_ref[...], w_ref[...])
        s = c * _sigmoid(c)
        if normalize:
            s = s * lax.rsqrt(jnp.sum(s * s, axis=-1, keepdims=True) + 1e-6) * scale
        o_ref[...] = s

    return pl.pallas_call(
        body, name=name, grid=(ncols,),
        in_specs=[pl.BlockSpec((t, HEAD_A), lambda j: (0, j + col0)), pl.BlockSpec((CONV_A, HEAD_A), lambda j: (0, j + col0))],
        out_specs=pl.BlockSpec((t, HEAD_A), lambda j: (0, j)), out_shape=jax.ShapeDtypeStruct((t, ncols * HEAD_A), F32),
        compiler_params=_params(("parallel",)))(h, w)


def _qkv_pre_bwd(h, w, dy, col0, ncols, normalize, scale, name):
    t = h.shape[0]

    def body(h_ref, w_ref, dy_ref, dh_ref, dw_ref):
        hv, wr, dyv = h_ref[...], w_ref[...], dy_ref[...]
        h1, h2, h3 = _shift_down(hv, 1), _shift_down(hv, 2), _shift_down(hv, 3)
        c = wr[3:4, :] * hv + wr[2:3, :] * h1 + wr[1:2, :] * h2 + wr[0:1, :] * h3
        s, dsilu = _silu_and_grad(c)
        if normalize:
            r = lax.rsqrt(jnp.sum(s * s, axis=-1, keepdims=True) + 1e-6)
            y = s * r
            dyv = dyv * scale
            ds = r * (dyv - y * jnp.sum(dyv * y, axis=-1, keepdims=True))
        else:
            ds = dyv
        dc = ds * dsilu
        dh = (wr[3:4, :] * dc + wr[2:3, :] * _shift_up(dc, 1) + wr[1:2, :] * _shift_up(dc, 2)
              + wr[0:1, :] * _shift_up(dc, 3))
        dh_ref[...] = dh.astype(BF16)
        dw_ref[0:1, :] = jnp.sum(dc * h3, axis=0, keepdims=True)
        dw_ref[1:2, :] = jnp.sum(dc * h2, axis=0, keepdims=True)
        dw_ref[2:3, :] = jnp.sum(dc * h1, axis=0, keepdims=True)
        dw_ref[3:4, :] = jnp.sum(dc * hv, axis=0, keepdims=True)

    return pl.pallas_call(
        body, name=name, grid=(ncols,),
        in_specs=[pl.BlockSpec((t, HEAD_A), lambda j: (0, j + col0)), pl.BlockSpec((CONV_A, HEAD_A), lambda j: (0, j + col0)),
                  pl.BlockSpec((t, HEAD_A), lambda j: (0, j))],
        out_specs=[pl.BlockSpec((t, HEAD_A), lambda j: (0, j)), pl.BlockSpec((CONV_A, HEAD_A), lambda j: (0, j))],
        out_shape=[jax.ShapeDtypeStruct((t, ncols * HEAD_A), BF16), jax.ShapeDtypeStruct((CONV_A, ncols * HEAD_A), F32)],
        compiler_params=_params(("parallel",)))(h, w, dy)


def _softplus(x):
    return jnp.maximum(x, 0.0) + jnp.log1p(jnp.exp(-jnp.abs(x)))


def _gates_fwd(ba, arow, brow, name):
    t = ba.shape[0]

    def body(x_ref, a_ref, b_ref, o_ref):
        xv = x_ref[...]
        lane = lax.broadcasted_iota(jnp.int32, xv.shape, 1)
        beta = _sigmoid(xv)
        g = -jnp.exp(a_ref[...]) * _softplus(xv + b_ref[...])
        o_ref[...] = jnp.where(lane < N_HEADS_A, beta, jnp.where(lane < 2 * N_HEADS_A, g, 0.0))

    return pl.pallas_call(body, name=name, out_shape=jax.ShapeDtypeStruct((t, LANE), F32),
                          compiler_params=_params())(ba, arow, brow)


def _gates_bwd(ba, arow, brow, dgb, name):
    t = ba.shape[0]

    def body(x_ref, a_ref, b_ref, d_ref, dx_ref, da_ref, db_ref):
        xv = x_ref[...]
        dv = d_ref[0] + d_ref[1] + d_ref[2] + d_ref[3]
        lane = lax.broadcasted_iota(jnp.int32, xv.shape, 1)
        beta = _sigmoid(xv)
        ea = jnp.exp(a_ref[...])
        z = xv + b_ref[...]
        dgv = jnp.where((lane >= N_HEADS_A) & (lane < 2 * N_HEADS_A), dv, 0.0) * (-ea)
        dz = dgv * _sigmoid(z)
        dx = jnp.where(lane < N_HEADS_A, dv * beta * (1.0 - beta), dz)
        dx_ref[...] = dx.astype(BF16)
        db_ref[...] = jnp.sum(dz, axis=0, keepdims=True)
        da_ref[...] = jnp.sum(dgv * _softplus(z), axis=0, keepdims=True)

    return pl.pallas_call(
        body, name=name,
        out_shape=[jax.ShapeDtypeStruct((t, LANE), BF16), jax.ShapeDtypeStruct((1, LANE), F32),
                   jax.ShapeDtypeStruct((1, LANE), F32)],
        compiler_params=_params())(ba, arow, brow, dgb)


def _dot(a, b, prec=None):
    if prec is None:
        return jnp.dot(a.astype(BF16), b.astype(BF16), preferred_element_type=F32)
    return jnp.dot(a, b, precision=prec, preferred_element_type=F32)


def _dot_nt(a, b, prec=None):
    if prec is None:
        a, b = a.astype(BF16), b.astype(BF16)
    return lax.dot_general(a, b, (((1,), (1,)), ((), ())), precision=prec, preferred_element_type=F32)


def _dot_tn(a, b, prec=None):
    if prec is None:
        a, b = a.astype(BF16), b.astype(BF16)
    return lax.dot_general(a, b, (((0,), (0,)), ((), ())), precision=prec, preferred_element_type=F32)


def _gdr_chunk_terms(k, beta, g):
    c = GDR_CHUNK
    row = lax.broadcasted_iota(jnp.int32, (c, c), 0)
    col = lax.broadcasted_iota(jnp.int32, (c, c), 1)
    causal, strict = row >= col, row > col
    gcum = _dot(causal.astype(F32), jnp.broadcast_to(g, (c, c)), HIGHEST)
    diff = gcum - gcum.T
    decay = jnp.where(causal, jnp.exp(jnp.where(causal, diff, 0.0)), 0.0)
    kb = k * beta
    kk = _dot_nt(kb, k)
    return row, col, causal, strict, gcum, decay, kb, kk


def _unit_lower_inverse(a):
    c = a.shape[0]
    eye = (lax.broadcasted_iota(jnp.int32, (c, c), 0) == lax.broadcasted_iota(jnp.int32, (c, c), 1)).astype(F32)
    p = -a
    inv = eye + p
    step = 1
    while 2 * step < c:
        p = _dot(p, p, HIGH)
        inv = inv + _dot(inv, p, HIGH)
        step *= 2
    return inv


def _head_gates(gates, head):
    lane = lax.broadcasted_iota(jnp.int32, gates.shape, 1)
    beta = jnp.sum(jnp.where(lane == head, gates, 0.0), axis=1, keepdims=True)
    g = jnp.sum(jnp.where(lane == head + N_HEADS_A, gates, 0.0), axis=1, keepdims=True)
    return beta, g


def _gdr_fwd(q, k, v, gates, name, comm=None):
    t = q.shape[0]
    c = GDR_CHUNK
    n = t // c

    hps = GDR_HEADS_PER_STEP

    def one_head(hh, q_ref, k_ref, v_ref, gb_ref, o_ref, tm_ref, s_ref, state):
        cols = slice(hh * HEAD_A, (hh + 1) * HEAD_A)
        qv, kv, vv = q_ref[:, cols], k_ref[:, cols], v_ref[:, cols]
        beta, g = _head_gates(gb_ref[...], pl.program_id(0) * hps + hh)
        row, col, causal, strict, gcum, decay, kb, kk = _gdr_chunk_terms(kv, beta, g)
        tm = _unit_lower_inverse(jnp.where(strict, kk * decay, 0.0))
        e = jnp.exp(gcum)
        u = _dot(tm, vv * beta, HIGH)
        w = _dot(tm, kb * e, HIGH)
        p = jnp.where(causal, _dot_nt(qv, kv) * decay, 0.0)
        s = state[hh]
        s_ref[hh, 0] = s
        tm_ref[hh, 0] = tm
        vn = u - _dot(w, s)
        o_ref[:, cols] = _dot(qv * e, s) + _dot(p, vn)
        glast = gcum[c - 1:c, :]
        state[hh] = s * jnp.exp(glast) + _dot_tn(kv * jnp.exp(glast - gcum), vn)

    def body(*refs):
        state = refs[-1]

        @pl.when(pl.program_id(1) == 0)
        def _():
            state[...] = jnp.zeros_like(state)

        for hh in range(hps):
            one_head(hh, *refs)

    blk = pl.BlockSpec((c, hps * HEAD_A), lambda h, i: (i, h))
    mat = pl.BlockSpec((hps, 1, c, c), lambda h, i: (h, i, 0, 0))
    return _call(
        body, name=name, grid=(N_HEADS_A // hps, n),
        in_specs=[blk, blk, blk, pl.BlockSpec((c, LANE), lambda h, i: (i, 0))],
        out_specs=[blk, mat, mat],
        out_shape=[jax.ShapeDtypeStruct((t, WIDTH_A), F32), jax.ShapeDtypeStruct((N_HEADS_A, n, c, c), F32),
                   jax.ShapeDtypeStruct((N_HEADS_A, n, HEAD_A, HEAD_A), F32)],
        scratch_shapes=[pltpu.VMEM((hps, HEAD_A, HEAD_A), F32)], sem=("parallel", "arbitrary"),
        args=(q, k, v, gates), comm=comm)


def _gdr_bwd(q, k, v, gates, tm_all, s_all, do, name, comm=None):
    t = q.shape[0]
    c = GDR_CHUNK
    n = t // c

    hps = GDR_HEADS_PER_STEP

    def one_head(hh, q_ref, k_ref, v_ref, gb_ref, tm_ref, s_ref, do_ref, dq_ref, dk_ref, dv_ref, dgb_ref, dstate):
        cols = slice(hh * HEAD_A, (hh + 1) * HEAD_A)
        qv, kv, vv, dov = q_ref[:, cols], k_ref[:, cols], v_ref[:, cols], do_ref[:, cols]
        head = pl.program_id(0) * hps + hh
        beta, g = _head_gates(gb_ref[...], head)
        tm, s, dsp = tm_ref[hh, 0], s_ref[hh, 0], dstate[hh]
        row, col, causal, strict, gcum, decay, kb, kk = _gdr_chunk_terms(kv, beta, g)
        e = jnp.exp(gcum)
        vb, kbe = vv * beta, kb * e
        u = _dot(tm, vb, HIGH)
        w = _dot(tm, kbe, HIGH)
        qk = _dot_nt(qv, kv)
        p = jnp.where(causal, qk * decay, 0.0)
        vn = u - _dot(w, s)
        glast = gcum[c - 1:c, :]
        el = jnp.exp(glast)
        f = jnp.exp(glast - gcum)
        kd = kv * f
        qe = qv * e

        dvn = _dot_tn(p, dov) + _dot(kd, dsp)
        dglast = el[:, 0:1] * jnp.sum(s * dsp, keepdims=True)
        dkd = _dot_nt(vn, dsp)
        dk = dkd * f
        df = jnp.sum(dkd * kv, axis=1, keepdims=True) * f[:, 0:1]
        dglast = dglast + jnp.sum(df, keepdims=True)
        dgc = -df
        dp = jnp.where(causal, _dot_nt(dov, vn), 0.0)
        dqe = _dot_nt(dov, s)
        dq = dqe * e
        de = jnp.sum(dqe * qv, axis=1, keepdims=True)
        dstate[hh] = dsp * el + _dot_tn(qe, dov) - _dot_tn(w, dvn)
        dw = -_dot_nt(dvn, s)
        dvb = _dot_tn(tm, dvn, HIGH)
        dkbe = _dot_tn(tm, dw, HIGH)
        da = -jnp.where(strict, _dot_nt(dvb, u) + _dot_nt(dkbe, w), 0.0)
        dkk = da * decay
        dqk = dp * decay
        dd = da * kk + dp * qk
        dq = dq + _dot(dqk, kv)
        dk = dk + _dot_tn(dqk, qv)
        dkb = _dot(dkk, kv) + dkbe * e
        dk = dk + _dot_tn(dkk, kb)
        de = de + jnp.sum(dkbe * kb, axis=1, keepdims=True)
        dk = dk + dkb * beta
        dbeta = jnp.sum(dkb * kv, axis=1, keepdims=True) + jnp.sum(dvb * vv, axis=1, keepdims=True)
        m = dd * decay
        dgc = dgc + jnp.sum(m, axis=1, keepdims=True) - jnp.sum(m.T, axis=1, keepdims=True)
        dgc = dgc + de * e[:, 0:1]
        dgc = dgc + jnp.where(row[:, 0:1] == c - 1, dglast, 0.0)
        dg = _dot((row <= col).astype(F32), jnp.broadcast_to(dgc, (c, c)), HIGHEST)
        dq_ref[:, cols] = dq
        dk_ref[:, cols] = dk
        dv_ref[:, cols] = dvb * beta
        lane = lax.broadcasted_iota(jnp.int32, (c, LANE), 1)
        dgb_ref[hh] = jnp.where(lane == head, dbeta, jnp.where(lane == head + N_HEADS_A, dg, 0.0))

    def body(*refs):
        dstate = refs[-1]

        @pl.when(pl.program_id(1) == 0)
        def _():
            dstate[...] = jnp.zeros_like(dstate)

        for hh in range(hps):
            one_head(hh, *refs)

    blk = pl.BlockSpec((c, hps * HEAD_A), lambda h, i: (n - 1 - i, h))
    mat = pl.BlockSpec((hps, 1, c, c), lambda h, i: (h, n - 1 - i, 0, 0))
    return _call(
        body, name=name, grid=(N_HEADS_A // hps, n),
        in_specs=[blk, blk, blk, pl.BlockSpec((c, LANE), lambda h, i: (n - 1 - i, 0)), mat, mat, blk],
        out_specs=[blk, blk, blk, pl.BlockSpec((hps, c, LANE), lambda h, i: (h, n - 1 - i, 0))],
        out_shape=[jax.ShapeDtypeStruct((t, WIDTH_A), F32)] * 3 + [jax.ShapeDtypeStruct((N_HEADS_A, t, LANE), F32)],
        scratch_shapes=[pltpu.VMEM((hps, HEAD_A, HEAD_A), F32)], sem=("parallel", "arbitrary"),
        args=(q, k, v, gates, tm_all, s_all, do), comm=comm)


_B_NN, _B_NT, _B_TN = ((2,), (1,)), ((2,), (2,)), ((1,), (1,))


def _bdot(a, b, dims=_B_NN, prec=None):
    if prec is None:
        a, b = a.astype(BF16), b.astype(BF16)
    return lax.dot_general(a, b, (dims, ((0,), (0,))), precision=prec, preferred_element_type=F32)


def _heads_of(ref):
    return jnp.stack([ref[:, h * HEAD_A:(h + 1) * HEAD_A] for h in range(N_HEADS_A)])


def _all_head_gates(gates):
    pairs = [_head_gates(gates, h) for h in range(N_HEADS_A)]
    return jnp.stack([b for b, _ in pairs]), jnp.stack([g for _, g in pairs])


def _gdr_terms(k, beta, g):
    h, c = k.shape[0], GDR_CHUNK
    row = lax.broadcasted_iota(jnp.int32, (c, c), 0)
    col = lax.broadcasted_iota(jnp.int32, (c, c), 1)
    causal, strict = row >= col, row > col
    lower = jnp.broadcast_to(causal.astype(F32), (h, c, c))
    gcum = _bdot(lower, jnp.broadcast_to(g, (h, c, c)), prec=HIGHEST)
    diff = gcum - jnp.swapaxes(gcum, 1, 2)
    decay = jnp.where(causal, jnp.exp(jnp.where(causal, diff, 0.0)), 0.0)
    kb = k * beta
    return row, col, causal, strict, gcum, decay, kb, _bdot(kb, k, _B_NT)


def _unit_lower_inverses(a):
    c = a.shape[1]
    eye = (lax.broadcasted_iota(jnp.int32, (c, c), 0) == lax.broadcasted_iota(jnp.int32, (c, c), 1)).astype(F32)
    p = -a
    inv = eye + p
    step = 1
    while 2 * step < c:
        p = _bdot(p, p, prec=HIGH)
        inv = inv + _bdot(inv, p, prec=HIGH)
        step *= 2
    return inv


def _gdr_fwd(q, k, v, gates, name, comm=None):
    t = q.shape[0]
    c, nh = GDR_CHUNK, N_HEADS_A
    n = t // c

    def body(q_ref, k_ref, v_ref, gb_ref, o_ref, tm_ref, s_ref, state):
        @pl.when(pl.program_id(0) == 0)
        def _():
            state[...] = jnp.zeros_like(state)

        qv, kv, vv = _heads_of(q_ref), _heads_of(k_ref), _heads_of(v_ref)
        beta, g = _all_head_gates(gb_ref[...])
        row, col, causal, strict, gcum, decay, kb, kk = _gdr_terms(kv, beta, g)
        tm = _unit_lower_inverses(jnp.where(strict, kk * decay, 0.0))
        e = jnp.exp(gcum)
        u = _bdot(tm, vv * beta, prec=HIGH)
        w = _bdot(tm, kb * e, prec=HIGH)
        p = jnp.where(causal, _bdot(qv, kv, _B_NT) * decay, 0.0)
        s = state[...]
        s_ref[:, 0] = s
        tm_ref[:, 0] = tm
        vn = u - _bdot(w, s)
        o = _bdot(qv * e, s) + _bdot(p, vn)
        for h in range(nh):
            o_ref[:, h * HEAD_A:(h + 1) * HEAD_A] = o[h]
        glast = gcum[:, c - 1:c, :]
        state[...] = s * jnp.exp(glast) + _bdot(kv * jnp.exp(glast - gcum), vn, _B_TN)

    blk = pl.BlockSpec((c, WIDTH_A), lambda i: (i, 0))
    mat = pl.BlockSpec((nh, 1, c, c), lambda i: (0, i, 0, 0))
    return _call(
        body, name=name, grid=(n,), in_specs=[blk, blk, blk, pl.BlockSpec((c, LANE), lambda i: (i, 0))],
        out_specs=[blk, mat, mat],
        out_shape=[jax.ShapeDtypeStruct((t, WIDTH_A), F32), jax.ShapeDtypeStruct((nh, n, c, c), F32),
                   jax.ShapeDtypeStruct((nh, n, HEAD_A, HEAD_A), F32)],
        scratch_shapes=[pltpu.VMEM((nh, HEAD_A, HEAD_A), F32)], sem=("arbitrary",),
        args=(q, k, v, gates), comm=comm)


def _gdr_bwd(q, k, v, gates, tm_all, s_all, do, name, comm=None):
    t = q.shape[0]
    c, nh = GDR_CHUNK, N_HEADS_A
    n = t // c

    def body(q_ref, k_ref, v_ref, gb_ref, tm_ref, s_ref, do_ref, dq_ref, dk_ref, dv_ref, dgb_ref, dstate):
        @pl.when(pl.program_id(0) == 0)
        def _():
            dstate[...] = jnp.zeros_like(dstate)

        qv, kv, vv, dov = _heads_of(q_ref), _heads_of(k_ref), _heads_of(v_ref), _heads_of(do_ref)
        beta, g = _all_head_gates(gb_ref[...])
        tm, s, dsp = tm_ref[:, 0], s_ref[:, 0], dstate[...]
        row, col, causal, strict, gcum, decay, kb, kk = _gdr_terms(kv, beta, g)
        rowsum = lambda x: jnp.sum(x, axis=2, keepdims=True)
        e = jnp.exp(gcum)
        vb, kbe = vv * beta, kb * e
        u = _bdot(tm, vb, prec=HIGH)
        w = _bdot(tm, kbe, prec=HIGH)
        qk = _bdot(qv, kv, _B_NT)
        p = jnp.where(causal, qk * decay, 0.0)
        vn = u - _bdot(w, s)
        glast = gcum[:, c - 1:c, :]
        el = jnp.exp(glast)
        f = jnp.exp(glast - gcum)
        kd = kv * f
        qe = qv * e

        dvn = _bdot(p, dov, _B_TN) + _bdot(kd, dsp)
        dglast = el[:, :, 0:1] * jnp.sum(s * dsp, axis=(1, 2), keepdims=True)
        dkd = _bdot(vn, dsp, _B_NT)
        dk = dkd * f
        df = rowsum(dkd * kv) * f[:, :, 0:1]
        dglast = dglast + jnp.sum(df, axis=1, keepdims=True)
        dgc = -df
        dp = jnp.where(causal, _bdot(dov, vn, _B_NT), 0.0)
        dqe = _bdot(dov, s, _B_NT)
        dq = dqe * e
        de = rowsum(dqe * qv)
        dstate[...] = dsp * el + _bdot(qe, dov, _B_TN) - _bdot(w, dvn, _B_TN)
        dw = -_bdot(dvn, s, _B_NT)
        dvb = _bdot(tm, dvn, _B_TN, prec=HIGH)
        dkbe = _bdot(tm, dw, _B_TN, prec=HIGH)
        da = -jnp.where(strict, _bdot(dvb, u, _B_NT) + _bdot(dkbe, w, _B_NT), 0.0)
        dkk = da * decay
        dqk = dp * decay
        dd = da * kk + dp * qk
        dq = dq + _bdot(dqk, kv)
        dk = dk + _bdot(dqk, qv, _B_TN)
        dkb = _bdot(dkk, kv) + dkbe * e
        dk = dk + _bdot(dkk, kb, _B_TN)
        de = de + rowsum(dkbe * kb)
        dk = dk + dkb * beta
        dbeta = rowsum(dkb * kv) + rowsum(dvb * vv)
        m = dd * decay
        dgc = dgc + rowsum(m) - rowsum(jnp.swapaxes(m, 1, 2))
        dgc = dgc + de * e[:, :, 0:1]
        dgc = dgc + jnp.where(row[:, 0:1] == c - 1, dglast, 0.0)
        upper = jnp.broadcast_to((row <= col).astype(F32), (nh, c, c))
        dg = _bdot(upper, jnp.broadcast_to(dgc, (nh, c, c)), prec=HIGHEST)
        dv = dvb * beta
        for h in range(nh):
            cols = slice(h * HEAD_A, (h + 1) * HEAD_A)
            dq_ref[:, cols] = dq[h]
            dk_ref[:, cols] = dk[h]
            dv_ref[:, cols] = dv[h]
        head = lax.broadcasted_iota(jnp.int32, (nh, c, LANE), 0)
        lane = lax.broadcasted_iota(jnp.int32, (nh, c, LANE), 2)
        dgb_ref[...] = jnp.where(lane == head, dbeta, jnp.where(lane == head + nh, dg, 0.0))

    blk = pl.BlockSpec((c, WIDTH_A), lambda i: (n - 1 - i, 0))
    mat = pl.BlockSpec((nh, 1, c, c), lambda i: (0, n - 1 - i, 0, 0))
    return _call(
        body, name=name, grid=(n,),
        in_specs=[blk, blk, blk, pl.BlockSpec((c, LANE), lambda i: (n - 1 - i, 0)), mat, mat, blk],
        out_specs=[blk, blk, blk, pl.BlockSpec((nh, c, LANE), lambda i: (0, n - 1 - i, 0))],
        out_shape=[jax.ShapeDtypeStruct((t, WIDTH_A), F32)] * 3 + [jax.ShapeDtypeStruct((nh, t, LANE), F32)],
        scratch_shapes=[pltpu.VMEM((nh, HEAD_A, HEAD_A), F32)], sem=("arbitrary",),
        args=(q, k, v, gates, tm_all, s_all, do), comm=comm)


def _onorm_fwd(o, gate, g, name):
    t = o.shape[0]

    def body(o_ref, gate_ref, g_ref, y_ref):
        ov, gv = o_ref[...], gate_ref[...]
        r = lax.rsqrt(jnp.mean(ov * ov, axis=-1, keepdims=True) + RMS_EPS)
        y_ref[...] = (ov * r * g_ref[...] * gv * _sigmoid(gv)).astype(BF16)

    blk = pl.BlockSpec((t, HEAD_A), lambda j: (0, j))
    return pl.pallas_call(
        body, name=name, grid=(N_HEADS_A,), in_specs=[blk, blk, pl.BlockSpec((1, HEAD_A), lambda j: (0, 0))],
        out_specs=blk, out_shape=jax.ShapeDtypeStruct((t, WIDTH_A), BF16),
        compiler_params=_params(("parallel",)))(o, gate, g)


def _onorm_bwd(o, gate, g, dy, name):
    t = o.shape[0]

    def body(o_ref, gate_ref, g_ref, dy_ref, do_ref, dgate_ref, dg_ref):
        @pl.when(pl.program_id(0) == 0)
        def _():
            dg_ref[...] = jnp.zeros_like(dg_ref)

        ov, gv, dyv = o_ref[...], gate_ref[...], dy_ref[...].astype(F32)
        r = lax.rsqrt(jnp.mean(ov * ov, axis=-1, keepdims=True) + RMS_EPS)
        oh = ov * r
        sg, dsg = _silu_and_grad(gv)
        dgate_ref[...] = (dyv * oh * g_ref[...] * dsg).astype(BF16)
        dn = dyv * sg
        dg_ref[...] += jnp.sum(dn * oh, axis=0, keepdims=True)
        dng = dn * g_ref[...]
        do_ref[...] = r * (dng - oh * jnp.mean(dng * oh, axis=-1, keepdims=True))

    blk = pl.BlockSpec((t, HEAD_A), lambda j: (0, j))
    vec = pl.BlockSpec((1, HEAD_A), lambda j: (0, 0))
    return pl.pallas_call(
        body, name=name, grid=(N_HEADS_A,), in_specs=[blk, blk, vec, blk], out_specs=[blk, blk, vec],
        out_shape=[jax.ShapeDtypeStruct((t, WIDTH_A), F32), jax.ShapeDtypeStruct((t, WIDTH_A), BF16),
                   jax.ShapeDtypeStruct((1, HEAD_A), F32)],
        compiler_params=_params(("arbitrary",)))(o, gate, g, dy)


def _cmul(ar, ai, br, bi):
    return ar * br - ai * bi, ar * bi + ai * br


def _scan_tables(ar, ai, reverse):
    p1 = (ar, ai)
    p2 = _cmul(*p1, *p1)
    p4 = _cmul(*p2, *p2)
    p8 = _cmul(*p4, *p4)
    p3 = _cmul(*p2, *p1)
    p5 = _cmul(*p4, *p1)
    p6 = _cmul(*p4, *p2)
    p7 = _cmul(*p4, *p3)
    pows = [p1, p2, p3, p4, p5, p6, p7, p8]
    rows = lax.broadcasted_iota(jnp.int32, (8, ar.shape[1]), 0)
    tr = jnp.zeros((8, ar.shape[1]), F32)
    ti = jnp.zeros((8, ar.shape[1]), F32)
    for r in range(8):
        pw = pows[7 - r] if reverse else pows[r]
        tr = jnp.where(rows == r, pw[0], tr)
        ti = jnp.where(rows == r, pw[1], ti)
    return p1, p2, p4, p8, tr, ti


def _tile_scan(xr, xi, p1, p2, p4, reverse):
    rows = lax.broadcasted_iota(jnp.int32, xr.shape, 0)
    for s, (pr, pi) in ((1, p1), (2, p2), (4, p4)):
        if reverse:
            keep = rows < 8 - s
            sr, si = pltpu.roll(xr, 8 - s, 0), pltpu.roll(xi, 8 - s, 0)
        else:
            keep = rows >= s
            sr, si = pltpu.roll(xr, s, 0), pltpu.roll(xi, s, 0)
        sr, si = jnp.where(keep, sr, 0.0), jnp.where(keep, si, 0.0)
        mr, mi = _cmul(pr, pi, sr, si)
        xr, xi = xr + mr, xi + mi
    return xr, xi


def _s5_scan_fwd(bu, a, name, tb=512, comm=None):
    t = bu.shape[0]
    cb = SCAN_CB
    nt = t // tb

    def body(b_ref, a_ref, x_ref, carry):
        @pl.when(pl.program_id(1) == 0)
        def _():
            carry[...] = jnp.zeros_like(carry)

        ar, ai = a_ref[:, 0:cb], a_ref[:, cb:2 * cb]
        p1, p2, p4, p8, tr, ti = _scan_tables(ar, ai, False)

        def step(j, c):
            cr, ci = c
            i = pl.multiple_of(j * 8, 8)
            xr, xi = _tile_scan(b_ref[pl.ds(i, 8), 0:cb], b_ref[pl.ds(i, 8), cb:2 * cb], p1, p2, p4, False)
            mr, mi = _cmul(tr, ti, cr, ci)
            xr, xi = xr + mr, xi + mi
            x_ref[pl.ds(i, 8), 0:cb] = xr
            x_ref[pl.ds(i, 8), cb:2 * cb] = xi
            return xr[7:8, :], xi[7:8, :]

        cr, ci = lax.fori_loop(0, tb // 8, step, (carry[0:1, :], carry[1:2, :]), unroll=2)
        carry[0:1, :] = cr
        carry[1:2, :] = ci

    blk = pl.BlockSpec((tb, 2 * cb), lambda j, i: (i, j))
    return _call(
        body, name=name, grid=(SSM_CH // cb, nt),
        in_specs=[blk, pl.BlockSpec((1, 2 * cb), lambda j, i: (0, j))], out_specs=blk,
        out_shape=jax.ShapeDtypeStruct((t, 2 * SSM_CH), F32), scratch_shapes=[pltpu.VMEM((8, cb), F32)],
        sem=("parallel", "arbitrary"), args=(bu, a), comm=comm)


def _s5_scan_bwd(dx, x, a, name, tb=512, comm=None):
    t = dx.shape[0]
    cb = SCAN_CB
    nt = t // tb
    nj = tb // 8

    def body(d_ref, x_ref, xp_ref, a_ref, l_ref, da_ref, carry, acc):
        tblk = pl.program_id(1)

        @pl.when(tblk == 0)
        def _():
            carry[...] = jnp.zeros_like(carry)
            acc[...] = jnp.zeros_like(acc)

        ar, ai = a_ref[:, 0:cb], a_ref[:, cb:2 * cb]
        p1, p2, p4, p8, tr, ti = _scan_tables(ar, -ai, True)
        rows = lax.broadcasted_iota(jnp.int32, (8, cb), 0)

        def step(jj, c):
            cr, ci, sr_acc, si_acc = c
            j = nj - 1 - jj
            i = pl.multiple_of(j * 8, 8)
            lr, li = _tile_scan(d_ref[pl.ds(i, 8), 0:cb], d_ref[pl.ds(i, 8), cb:2 * cb], p1, p2, p4, True)
            mr, mi = _cmul(tr, ti, cr, ci)
            lr, li = lr + mr, li + mi
            l_ref[pl.ds(i, 8), 0:cb] = lr
            l_ref[pl.ds(i, 8), cb:2 * cb] = li
            ip = pl.multiple_of(jnp.maximum(j - 1, 0) * 8, 8)
            prev_r = jnp.where(j > 0, x_ref[pl.ds(ip, 8), 0:cb], xp_ref[:, 0:cb])
            prev_i = jnp.where(j > 0, x_ref[pl.ds(ip, 8), cb:2 * cb], xp_ref[:, cb:2 * cb])
            edge = jnp.where(jnp.logical_and(j == 0, tblk == nt - 1), 0.0, 1.0)
            xs_r = jnp.where(rows == 0, pltpu.roll(prev_r, 1, 0) * edge, pltpu.roll(x_ref[pl.ds(i, 8), 0:cb], 1, 0))
            xs_i = jnp.where(rows == 0, pltpu.roll(prev_i, 1, 0) * edge, pltpu.roll(x_ref[pl.ds(i, 8), cb:2 * cb], 1, 0))
            sr_acc = sr_acc + lr * xs_r + li * xs_i
            si_acc = si_acc + li * xs_r - lr * xs_i
            return lr[0:1, :], li[0:1, :], sr_acc, si_acc

        cr, ci, sr_acc, si_acc = lax.fori_loop(
            0, nj, step, (carry[0:1, :], carry[1:2, :], acc[:, 0:cb], acc[:, cb:2 * cb]))
        carry[0:1, :] = cr
        carry[1:2, :] = ci
        acc[:, 0:cb] = sr_acc
        acc[:, cb:2 * cb] = si_acc

        @pl.when(tblk == nt - 1)
        def _():
            da_ref[...] = jnp.sum(acc[...], axis=0, keepdims=True)

    blk = pl.BlockSpec((tb, 2 * cb), lambda j, i: (nt - 1 - i, j))
    prev = pl.BlockSpec((8, 2 * cb), lambda j, i: (jnp.maximum((nt - 1 - i) * (tb // 8) - 1, 0), j))
    vec = pl.BlockSpec((1, 2 * cb), lambda j, i: (0, j))
    return _call(
        body, name=name, grid=(SSM_CH // cb, nt), in_specs=[blk, blk, prev, vec], out_specs=[blk, vec],
        out_shape=[jax.ShapeDtypeStruct((t, 2 * SSM_CH), F32), jax.ShapeDtypeStruct((1, 2 * SSM_CH), F32)],
        scratch_shapes=[pltpu.VMEM((8, cb), F32), pltpu.VMEM((8, 2 * cb), F32)],
        sem=("parallel", "arbitrary"), args=(dx, x, x, a), comm=comm)


def _glu_fwd(yc, u, dvec, wg, bg, name, tr=256):
    t = yc.shape[0]

    def body(yc_ref, u_ref, d_ref, w_ref, b_ref, yl_ref, yb_ref):
        yl = yc_ref[...] + d_ref[...] * u_ref[...]
        yl_ref[...] = yl
        yg, _ = _gelu_and_grad(yl)
        z = jnp.dot(yg.astype(BF16), w_ref[...], preferred_element_type=F32) + b_ref[...]
        yb_ref[...] = (yg * _sigmoid(z)).astype(BF16)

    blk = pl.BlockSpec((tr, SSM_WIDTH), lambda i: (i, 0))
    vec = pl.BlockSpec((1, SSM_WIDTH), lambda i: (0, 0))
    return pl.pallas_call(
        body, name=name, grid=(t // tr,),
        in_specs=[blk, blk, vec, pl.BlockSpec((SSM_WIDTH, SSM_WIDTH), lambda i: (0, 0)), vec],
        out_specs=[blk, blk],
        out_shape=[jax.ShapeDtypeStruct((t, SSM_WIDTH), F32), jax.ShapeDtypeStruct((t, SSM_WIDTH), BF16)],
        compiler_params=_params(("parallel",)))(yc, u, dvec, wg, bg)


def _glu_bwd(yl, u, dvec, wg, bg, dyb, name, tr=256):
    t = yl.shape[0]

    def body(yl_ref, u_ref, d_ref, w_ref, b_ref, dy_ref, dyl_ref, du_ref, dw_ref, db_ref, dd_ref):
        @pl.when(pl.program_id(0) == 0)
        def _():
            dw_ref[...] = jnp.zeros_like(dw_ref)
            db_ref[...] = jnp.zeros_like(db_ref)
            dd_ref[...] = jnp.zeros_like(dd_ref)

        ylv, dyv, wv = yl_ref[...], dy_ref[...].astype(F32), w_ref[...]
        yg, dgelu = _gelu_and_grad(ylv)
        ygb = yg.astype(BF16)
        z = jnp.dot(ygb, wv, preferred_element_type=F32) + b_ref[...]
        sg = _sigmoid(z)
        dz = dyv * yg * sg * (1.0 - sg)
        dzb = dz.astype(BF16)
        dyg = dyv * sg + lax.dot_general(dzb, wv, (((1,), (1,)), ((), ())), preferred_element_type=F32)
        dyl = dyg * dgelu
        dyl_ref[...] = dyl.astype(BF16)
        du_ref[...] = dyl * d_ref[...]
        dw_ref[...] += lax.dot_general(ygb, dzb, (((0,), (0,)), ((), ())), preferred_element_type=F32)
        db_ref[...] += jnp.sum(dz, axis=0, keepdims=True)
        dd_ref[...] += jnp.sum(dyl * u_ref[...], axis=0, keepdims=True)

    blk = pl.BlockSpec((tr, SSM_WIDTH), lambda i: (i, 0))
    vec = pl.BlockSpec((1, SSM_WIDTH), lambda i: (0, 0))
    wsp = pl.BlockSpec((SSM_WIDTH, SSM_WIDTH), lambda i: (0, 0))
    return pl.pallas_call(
        body, name=name, grid=(t // tr,), in_specs=[blk, blk, vec, wsp, vec, blk],
        out_specs=[blk, blk, wsp, vec, vec],
        out_shape=[jax.ShapeDtypeStruct((t, SSM_WIDTH), BF16), jax.ShapeDtypeStruct((t, SSM_WIDTH), F32),
                   jax.ShapeDtypeStruct((SSM_WIDTH, SSM_WIDTH), F32), jax.ShapeDtypeStruct((1, SSM_WIDTH), F32),
                   jax.ShapeDtypeStruct((1, SSM_WIDTH), F32)],
        compiler_params=_params(("arbitrary",)))(yl, u, dvec, wg, bg, dyb)


def _mesh_pos():
    return lax.axis_index("x"), lax.axis_index("y"), lax.axis_index("c")


def _device_index():
    x, y, c = _mesh_pos()
    return 4 * x + 2 * y + c


def _gather_comm(arrays):
    na = len(arrays)

    def own_copy(ins, outs, sems, ai):
        return pltpu.make_async_copy(ins[ai], outs[ai].at[_device_index()], sems[2].at[ai])

    def ctx(ins, outs, sems):
        send_sems, recv_sems = sems[:2]
        x, y, c = _mesh_pos()
        chips = [(1 - x, y), (x, 1 - y), (1 - x, 1 - y)]

        def copy(ai, kk, block, to, own=False):
            slot = outs[ai].at[4 * block[0] + 2 * block[1] + block[2]]
            return pltpu.make_async_remote_copy(
                src_ref=ins[ai] if own else slot, dst_ref=slot, send_sem=send_sems.at[ai, kk],
                recv_sem=recv_sems.at[ai, kk], device_id=to, device_id_type=MESH)

        return (x, y, c), (x, y, 1 - c), chips, c, copy

    def start(ins, outs, sems):
        me, sibling, chips, c, copy = ctx(ins, outs, sems)
        for ai in range(na):
            copy(ai, 0, me, sibling, own=True).start()
            for j, chip in enumerate(chips):
                copy(ai, 1 + j, me, (*chip, c), own=True).start()
        for ai in range(na):
            own_copy(ins, outs, sems, ai).start()

    def mid(ins, outs, sems):
        me, sibling, chips, c, copy = ctx(ins, outs, sems)
        for ai in range(na):
            for j, chip in enumerate(chips):
                copy(ai, 1 + j, (*chip, c), me).wait_recv()
                copy(ai, 4 + j, (*chip, c), sibling).start()

    def end(ins, outs, sems):
        me, sibling, chips, c, copy = ctx(ins, outs, sems)
        for ai in range(na):
            copy(ai, 0, sibling, me).wait_recv()
            copy(ai, 0, me, sibling, own=True).wait_send()
            for j, chip in enumerate(chips):
                copy(ai, 4 + j, (*chip, 1 - c), me).wait_recv()
                copy(ai, 1 + j, me, (*chip, c), own=True).wait_send()
                copy(ai, 4 + j, (*chip, c), sibling).wait_send()
            own_copy(ins, outs, sems, ai).wait()

    return Comm(arrays, [jax.ShapeDtypeStruct((N_DEV,) + a.shape, a.dtype) for a in arrays],
                [pltpu.SemaphoreType.DMA((na, 7)), pltpu.SemaphoreType.DMA((na, 7)), pltpu.SemaphoreType.DMA((na,))],
                start, end, mid)


def _sequencer_gather(arrays, name, collective_id):
    comm = _gather_comm(arrays)
    na = len(arrays)

    def body(*refs):
        ins, outs, sems = refs[:na], refs[na:2 * na], refs[2 * na:]
        x, y, c = _mesh_pos()
        peers = [(x, y, 1 - c), (1 - x, y, c), (x, 1 - y, c), (1 - x, 1 - y, c)]
        barrier = pltpu.get_barrier_semaphore()
        for peer in peers:
            pl.semaphore_signal(barrier, inc=1, device_id=peer, device_id_type=MESH)
        pl.semaphore_wait(barrier, len(peers))
        comm.start(ins, outs, sems)
        comm.mid(ins, outs, sems)
        comm.end(ins, outs, sems)

    return list(pl.kernel(
        body, out_type=tuple(comm.out_shapes), mesh=plsc.ScalarSubcoreMesh(axis_name="sequencer", num_cores=1),
        name=name, scratch_types=tuple(comm.sems),
        compiler_params=pltpu.CompilerParams(collective_id=collective_id))(*arrays))


def _sequencer_exchange(comm, peers_of, name, collective_id):
    na = len(comm.inputs)

    def body(*refs):
        ins, outs, sems = refs[:na], refs[na:na + len(comm.out_shapes)], refs[na + len(comm.out_shapes):]
        peers = peers_of(*_mesh_pos())
        barrier = pltpu.get_barrier_semaphore()
        for peer in peers:
            pl.semaphore_signal(barrier, inc=1, device_id=peer, device_id_type=MESH)
        pl.semaphore_wait(barrier, len(peers))
        comm.start(ins, outs, sems)
        comm.end(ins, outs, sems)

    return list(pl.kernel(
        body, out_type=tuple(comm.out_shapes), mesh=plsc.ScalarSubcoreMesh(axis_name="sequencer", num_cores=1),
        name=name, scratch_types=tuple(comm.sems),
        compiler_params=pltpu.CompilerParams(collective_id=collective_id))(*comm.inputs))


SIBLING_SWAP_ID, CHIP_EXCHANGE_ID = 9, 10


def _sequencer_swap(arrays, name):
    return _sequencer_exchange(_swap_comm(arrays), lambda x, y, c: [(x, y, 1 - c)], name, SIBLING_SWAP_ID)[0]


def _sequencer_chips(send, name):
    return _sequencer_exchange(_chips_comm(send), lambda x, y, c: [(1 - x, y, c), (x, 1 - y, c), (1 - x, 1 - y, c)],
                               name, CHIP_EXCHANGE_ID)[0]


def _swap_comm(arrays):
    na = len(arrays)
    offs = np.concatenate([[0], np.cumsum([a.shape[1] for a in arrays])]).astype(int)

    def copies(ins, outs, sems):
        x, y, c = _mesh_pos()
        return [pltpu.make_async_remote_copy(
            src_ref=ins[ai].at[2 * k + 1 - c], dst_ref=outs[0].at[k, pl.ds(int(offs[ai]), arrays[ai].shape[1])],
            send_sem=sems[0].at[ai, k], recv_sem=sems[1].at[ai, k], device_id=(x, y, 1 - c), device_id_type=MESH)
            for ai in range(na) for k in range(4)]

    def start(ins, outs, sems):
        for cp in copies(ins, outs, sems):
            cp.start()

    def end(ins, outs, sems):
        for cp in copies(ins, outs, sems):
            cp.wait()

    return Comm(arrays, [jax.ShapeDtypeStruct((4, int(offs[-1]), PACK_COLS), arrays[0].dtype)],
                [pltpu.SemaphoreType.DMA((na, 4)), pltpu.SemaphoreType.DMA((na, 4))], start, end)


def _chips_comm(send):
    def copies(ins, outs, sems):
        x, y, c = _mesh_pos()
        chips = [(1 - x, y), (x, 1 - y), (1 - x, 1 - y)]
        return [pltpu.make_async_remote_copy(
            src_ref=ins[0].at[2 * cx + cy], dst_ref=outs[0].at[j], send_sem=sems[0].at[j], recv_sem=sems[1].at[j],
            device_id=(cx, cy, c), device_id_type=MESH) for j, (cx, cy) in enumerate(chips)]

    def start(ins, outs, sems):
        for cp in copies(ins, outs, sems):
            cp.start()

    def end(ins, outs, sems):
        for cp in copies(ins, outs, sems):
            cp.wait()

    return Comm([send], [jax.ShapeDtypeStruct((3,) + send.shape[1:], send.dtype)],
                [pltpu.SemaphoreType.DMA((3,)), pltpu.SemaphoreType.DMA((3,))], start, end)


def _all_gather(arrays, name):
    na = len(arrays)

    def body(*refs):
        ins, outs = refs[:na], refs[na:2 * na]
        send_sems, recv_sems, local_sems = refs[2 * na:]
        x, y, c = _mesh_pos()
        me, sibling = (x, y, c), (x, y, 1 - c)
        chips = [(1 - x, y), (x, 1 - y), (1 - x, 1 - y)]
        waits = []
        for ai in range(na):
            in_ref, out_ref = ins[ai], outs[ai]

            def slot(px, py, pc, out_ref=out_ref):
                return out_ref.at[4 * px + 2 * py + pc]

            def copy(kk, block, to, src=None, ai=ai, slot=slot):
                return pltpu.make_async_remote_copy(
                    src_ref=slot(*block) if src is None else src, dst_ref=slot(*block),
                    send_sem=send_sems.at[ai, kk], recv_sem=recv_sems.at[ai, kk], device_id=to, device_id_type=MESH)

            mine = pltpu.make_async_copy(in_ref, slot(*me), local_sems.at[ai])
            mine.start()
            first = [copy(0, me, sibling, src=in_ref)]
            first += [copy(1 + j, me, (*chip, c), src=in_ref) for j, chip in enumerate(chips)]
            for cp in first:
                cp.start()
            waits.append((copy, mine, first))
        sends = []
        for ai in range(na):
            copy, mine, first = waits[ai]
            passed = [copy(4 + j, (*chip, c), sibling) for j, chip in enumerate(chips)]
            for j, chip in enumerate(chips):
                copy(1 + j, (*chip, c), me).wait_recv()
                passed[j].start()
            sends.append(passed)
        for ai in range(na):
            copy, mine, first = waits[ai]
            copy(0, sibling, me).wait_recv()
            for j, chip in enumerate(chips):
                copy(4 + j, (*chip, 1 - c), me).wait_recv()
            for cp in first + sends[ai]:
                cp.wait_send()
            mine.wait()

    any_spec = pl.BlockSpec(memory_space=pl.ANY)
    return pl.pallas_call(
        body, name=name, in_specs=[any_spec] * na, out_specs=[any_spec] * na,
        out_shape=[jax.ShapeDtypeStruct((N_DEV,) + a.shape, a.dtype) for a in arrays],
        scratch_shapes=[pltpu.SemaphoreType.DMA((na, 7)), pltpu.SemaphoreType.DMA((na, 7)),
                        pltpu.SemaphoreType.DMA((na,))],
        compiler_params=pltpu.CompilerParams(has_side_effects=True))(*arrays)


def _swap_sibling(arrays, name):
    na = len(arrays)
    offs = np.concatenate([[0], np.cumsum([a.shape[1] for a in arrays])]).astype(int)
    rows = int(offs[-1])

    def body(*refs):
        ins, recv_ref = refs[:na], refs[na]
        send_sems, recv_sems = refs[na + 1:]
        x, y, c = _mesh_pos()
        started = []
        for ai in range(na):
            span = pl.ds(int(offs[ai]), arrays[ai].shape[1])
            for k in range(4):
                remote = pltpu.make_async_remote_copy(
                    src_ref=ins[ai].at[2 * k + 1 - c], dst_ref=recv_ref.at[k, span], send_sem=send_sems.at[ai, k],
                    recv_sem=recv_sems.at[ai, k], device_id=(x, y, 1 - c), device_id_type=MESH)
                remote.start()
                started.append(remote)
        for remote in started:
            remote.wait()

    any_spec = pl.BlockSpec(memory_space=pl.ANY)
    return pl.pallas_call(
        body, name=name, in_specs=[any_spec] * na, out_specs=any_spec,
        out_shape=jax.ShapeDtypeStruct((4, rows, PACK_COLS), arrays[0].dtype),
        scratch_shapes=[pltpu.SemaphoreType.DMA((na, 4)), pltpu.SemaphoreType.DMA((na, 4))])(*arrays)


def _exchange_chips(send, name):
    def body(s_ref, o_ref, send_sems, recv_sems):
        x, y, c = _mesh_pos()
        chips = [(1 - x, y), (x, 1 - y), (1 - x, 1 - y)]
        cps = [pltpu.make_async_remote_copy(
            src_ref=s_ref.at[2 * cx + cy], dst_ref=o_ref.at[j], send_sem=send_sems.at[j], recv_sem=recv_sems.at[j],
            device_id=(cx, cy, c), device_id_type=MESH) for j, (cx, cy) in enumerate(chips)]
        for cp in cps:
            cp.start()
        for cp in cps:
            cp.wait()

    any_spec = pl.BlockSpec(memory_space=pl.ANY)
    return pl.pallas_call(
        body, name=name, in_specs=[any_spec], out_specs=any_spec,
        out_shape=jax.ShapeDtypeStruct((3,) + send.shape[1:], send.dtype),
        scratch_shapes=[pltpu.SemaphoreType.DMA((3,)), pltpu.SemaphoreType.DMA((3,))])(send)


def _pair_sum(keep, recv, name, tr=464):
    nchip, rows, cols = keep.shape

    def body(g_ref, r_ref, o_ref):
        o_ref[...] = (g_ref[...].astype(F32) + r_ref[...].astype(F32)).astype(BF16)

    blk = pl.BlockSpec((1, tr, cols), lambda k, i: (k, i, 0))
    return pl.pallas_call(
        body, name=name, grid=(nchip, rows // tr), in_specs=[blk, blk], out_specs=blk,
        out_shape=jax.ShapeDtypeStruct((nchip, rows, cols), BF16),
        compiler_params=_params(("parallel", "parallel")))(keep, recv)


def _pair_sum_pieces(pieces, recv, name, tr):
    _, rows, cols = pieces.shape
    core = lax.axis_index("c").astype(jnp.int32).reshape(1)

    def body(c_ref, g_ref, r_ref, o_ref):
        del c_ref
        o_ref[...] = (g_ref[...].astype(F32) + r_ref[...].astype(F32)).astype(BF16)

    grid_spec = pltpu.PrefetchScalarGridSpec(
        num_scalar_prefetch=1, grid=(4, rows // tr),
        in_specs=[pl.BlockSpec((1, tr, cols), lambda k, i, c_ref: (2 * k + c_ref[0], i, 0)),
                  pl.BlockSpec((1, tr, cols), lambda k, i, c_ref: (k, i, 0))],
        out_specs=pl.BlockSpec((1, tr, cols), lambda k, i, c_ref: (k, i, 0)))
    return pl.pallas_call(
        body, name=name, grid_spec=grid_spec, out_shape=jax.ShapeDtypeStruct((4, rows, cols), BF16),
        compiler_params=_params(("parallel", "parallel")))(core, pieces, recv)


def _chip_sum(own, others, name, tr=464):
    _, rows, cols = own.shape
    chip = (2 * lax.axis_index("x") + lax.axis_index("y")).astype(jnp.int32).reshape(1)

    def body(chip_ref, own_ref, oth_ref, o_ref):
        del chip_ref
        acc = own_ref[0].astype(F32)
        for j in range(3):
            acc = acc + oth_ref[j].astype(F32)
        o_ref[...] = acc

    grid_spec = pltpu.PrefetchScalarGridSpec(
        num_scalar_prefetch=1, grid=(rows // tr,),
        in_specs=[pl.BlockSpec((1, tr, cols), lambda i, chip_ref: (chip_ref[0], i, 0)),
                  pl.BlockSpec((3, tr, cols), lambda i, chip_ref: (0, i, 0))],
        out_specs=pl.BlockSpec((tr, cols), lambda i, chip_ref: (i, 0)))
    return pl.pallas_call(
        body, name=name, grid_spec=grid_spec, out_shape=jax.ShapeDtypeStruct((rows, cols), F32),
        compiler_params=_params(("parallel",)))(chip, own, others)


def _sum_leading(parts, name, tr=464):
    nparts, rows, cols = parts.shape
    tr = tr if rows % tr == 0 else rows

    def body(p_ref, o_ref):
        acc = p_ref[0].astype(F32)
        for i in range(1, nparts):
            acc = acc + p_ref[i].astype(F32)
        o_ref[...] = acc

    return pl.pallas_call(
        body, name=name, grid=(rows // tr,),
        in_specs=[pl.BlockSpec((nparts, tr, cols), lambda i: (0, i, 0))],
        out_specs=pl.BlockSpec((tr, cols), lambda i: (i, 0)), out_shape=jax.ShapeDtypeStruct((rows, cols), F32),
        compiler_params=_params(("parallel",)))(parts)


def _adamw(w, g, m, v, name, comm=None):
    shape = w.shape
    cols = shape[-1]
    lead = shape[0] if len(shape) >= 3 else 1
    rows = int(np.prod(shape[:-1])) // lead if len(shape) > 1 else 1
    w2, g2, m2, v2 = (a.reshape(lead, rows, cols) for a in (w, g, m, v))
    tr = rows
    for cand in (512, 256, 128, 64, 32, 16, 8):
        if rows % cand == 0 and rows > cand:
            tr = cand
            break
    bc1, bc2 = 1.0 - ADAM_B1 ** ADAM_STEP, 1.0 - ADAM_B2 ** ADAM_STEP

    def body(w_ref, g_ref, m_ref, v_ref, d_ref, nm_ref, nv_ref):
        gv = g_ref[...]
        nm = ADAM_B1 * m_ref[...] + (1.0 - ADAM_B1) * gv
        nv = ADAM_B2 * v_ref[...] + (1.0 - ADAM_B2) * (gv * gv)
        nm_ref[...] = nm
        nv_ref[...] = nv
        d_ref[...] = -ADAM_LR * ((nm / bc1) / (jnp.sqrt(nv / bc2) + ADAM_EPS) + ADAM_WD * w_ref[...])

    blk = pl.BlockSpec((1, tr, cols), lambda l, i: (l, i, 0))
    res = _call(body, name=name, grid=(lead, rows // tr), in_specs=[blk] * 4, out_specs=[blk] * 3,
                out_shape=[jax.ShapeDtypeStruct((lead, rows, cols), F32)] * 3, sem=("parallel", "parallel"),
                args=(w2, g2, m2, v2), comm=comm)
    outs, couts = res if comm is not None else (res, None)
    outs = tuple(o.reshape(shape) for o in outs)
    return outs if comm is None else (outs, couts)


WEIGHT_NAMES = ['norm_mix_g', 'norm_xa_g', 'norm_ffn_g', 'norm_mem_g', 'norm_final_g', 'w_in_ab', 'conv_qkv_a',
                'a_log_a', 'dt_bias_a', 'onorm_g_a', 'ssm_lambda_re', 'ssm_lambda_im', 'ssm_b_re', 'ssm_b_im',
                'ssm_c_re', 'ssm_c_im', 'ssm_d', 'ssm_log_dt', 'w_glu_b', 'b_glu_b', 'w_out_ab', 'pool_w',
                'pool_scale', 'xa_wq', 'xa_wkv', 'xa_wo', 'ffn_w_up', 'ffn_conv', 'ffn_w_down']
BIG_SHARDED = {'w_in_ab': ((1, 1024, 2568), 2), 'w_glu_b': ((1, 512, 512), 1), 'w_out_ab': ((1, 1024, 1024), 1),
               'pool_w': ((1, 4, 256, 256), 2), 'xa_wq': ((2, 1024, 1024), 1), 'xa_wkv': ((2, 1024, 2048), 2),
               'xa_wo': ((2, 1024, 1024), 1), 'ffn_w_up': ((2, 1024, 5632), 2), 'ffn_w_down': ((2, 2816, 1024), 1)}
SMALL_SHARDED = {'conv_qkv_a': ((1, 4, 1536), 2), 'pool_scale': ((1, 1024), 1), 'ffn_conv': ((2, 3, 5632), 2)}
REPLICATED = {'norm_mix_g': (2, 1024), 'norm_xa_g': (2, 1024), 'norm_ffn_g': (2, 1024), 'norm_mem_g': (1024,),
              'norm_final_g': (1024,), 'a_log_a': (1, 4), 'dt_bias_a': (1, 4), 'onorm_g_a': (1, 128),
              'ssm_lambda_re': (1, 32, 64), 'ssm_lambda_im': (1, 32, 64), 'ssm_b_re': (1, 32, 64, 16),
              'ssm_b_im': (1, 32, 64, 16), 'ssm_c_re': (1, 32, 16, 64), 'ssm_c_im': (1, 32, 16, 64),
              'ssm_d': (1, 32, 16), 'ssm_log_dt': (1, 32), 'b_glu_b': (1, 512)}
PACK_ROW_ALIGN = 8


def _shard_shape(shape, axis):
    return tuple(s // N_DEV if i == axis else s for i, s in enumerate(shape))


def _round_up(n, m):
    return (n + m - 1) // m * m


def _pack(arrays):
    total = sum(int(np.prod(a.shape)) for a in arrays)
    padded = _round_up(total, PACK_COLS * PACK_ROW_ALIGN)
    parts = [a.astype(F32).reshape(-1) for a in arrays]
    if padded != total:
        parts.append(jnp.zeros((padded - total,), F32))
    return jnp.concatenate(parts).reshape(padded // PACK_COLS, PACK_COLS)


def _unpack(packed, shapes):
    flat, out, off = packed.reshape(-1), [], 0
    for shape in shapes:
        size = int(np.prod(shape))
        out.append(flat[off:off + size].reshape(shape))
        off += size
    return out


def _split_shards(full, axis):
    shape = full.shape
    s = shape[axis] // N_DEV
    a = full.reshape(shape[:axis] + (N_DEV, s) + shape[axis + 1:])
    return jnp.moveaxis(a, axis, 0).reshape(N_DEV, -1)


def _merge_shards(pieces, shape, axis):
    sh = _shard_shape(shape, axis)
    a = pieces.reshape((N_DEV,) + sh)
    a = jnp.moveaxis(a, 0, axis)
    return a.reshape(shape)


_SCAN_NB = SSM_CH // SCAN_CB


def _to_scan_layout(m, axis):
    shape = m.shape
    m = m.reshape(shape[:axis] + (2, _SCAN_NB, SCAN_CB) + shape[axis + 1:])
    return jnp.swapaxes(m, axis, axis + 1).reshape(shape)


def _from_scan_layout(m, axis):
    shape = m.shape
    m = m.reshape(shape[:axis] + (_SCAN_NB, 2, SCAN_CB) + shape[axis + 1:])
    return jnp.swapaxes(m, axis, axis + 1).reshape(shape)


def _s5_discretise(lam_re, lam_im, b_re, b_im, log_dt):
    dt = jnp.exp(log_dt)[:, None]
    mag = jnp.exp(lam_re * dt)
    ang = lam_im * dt
    lb_re, lb_im = mag * jnp.cos(ang), mag * jnp.sin(ang)
    den = lam_re * lam_re + lam_im * lam_im
    nr, ni = lb_re - 1.0, lb_im
    coef_re = (nr * lam_re + ni * lam_im) / den
    coef_im = (ni * lam_re - nr * lam_im) / den
    bb_re = coef_re[..., None] * b_re - coef_im[..., None] * b_im
    bb_im = coef_re[..., None] * b_im + coef_im[..., None] * b_re
    return lb_re, lb_im, bb_re, bb_im


_GROUPS_PER_BLOCK = N_GROUPS // _SCAN_NB
_U_BLOCK = _GROUPS_PER_BLOCK * SSM_GROUP


def _s5_matrices(lb_re, lb_im, bb_re, bb_im, c_re, c_im):
    eye = jnp.eye(_GROUPS_PER_BLOCK, dtype=F32)
    blocked = lambda m: m.reshape((_SCAN_NB, _GROUPS_PER_BLOCK) + m.shape[1:])
    bmat = lambda bb: jnp.einsum('jgph,gk->jghkp', blocked(bb), eye).reshape(_SCAN_NB, _U_BLOCK, SCAN_CB)
    cmat = lambda cc: jnp.einsum('jghp,gk->jkpgh', blocked(cc), eye).reshape(_SCAN_NB, SCAN_CB, _U_BLOCK)
    b_in = jnp.concatenate([bmat(bb_re), bmat(bb_im)], axis=2)
    c_out = jnp.concatenate([cmat(c_re), -cmat(c_im)], axis=1)
    a_row = _to_scan_layout(jnp.concatenate([lb_re.reshape(1, SSM_CH), lb_im.reshape(1, SSM_CH)], axis=1), 1)
    return b_in, c_out, a_row


def _s5_matrix_grads(db_in, dc_out, da_row):
    da_nat = _from_scan_layout(da_row, 1)
    eye = jnp.eye(_GROUPS_PER_BLOCK, dtype=F32)
    nb, gb = _SCAN_NB, _GROUPS_PER_BLOCK
    bgrad = lambda m: jnp.einsum('jghkp,gk->jgph', m.reshape(nb, gb, SSM_GROUP, gb, SSM_STATE), eye
                                 ).reshape(N_GROUPS, SSM_STATE, SSM_GROUP)
    cgrad = lambda m: jnp.einsum('jkpgh,gk->jghp', m.reshape(nb, gb, SSM_STATE, gb, SSM_GROUP), eye
                                 ).reshape(N_GROUPS, SSM_GROUP, SSM_STATE)
    dbb_re, dbb_im = bgrad(db_in[:, :, :SCAN_CB]), bgrad(db_in[:, :, SCAN_CB:])
    dc_re, dc_im = cgrad(dc_out[:, :SCAN_CB]), -cgrad(dc_out[:, SCAN_CB:])
    dlb_re = da_nat[0, :SSM_CH].reshape(N_GROUPS, SSM_STATE)
    dlb_im = da_nat[0, SSM_CH:].reshape(N_GROUPS, SSM_STATE)
    return dlb_re, dlb_im, dbb_re, dbb_im, dc_re, dc_im


def _as_pieces(a):
    return a.reshape(N_DEV, a.shape[0] // N_DEV, a.shape[1])


def _hybrid_fwd(xn, x, wts, p, weights, riders):
    sv = {}
    hq = _mm(xn, wts['w_qkv_t'], "nt", "l0_in_qkv")
    gate = _mm(xn, wts['w_gate_t'], "nt", "l0_in_gate")
    ba = _mm(xn, wts['w_ba_t'], "nt", "l0_in_ba")
    u = _mm(xn, wts['w_u_t'], "nt", "l0_in_u")
    conv = p['conv_qkv']
    q = _qkv_pre_fwd(hq, conv, 0, 4, True, HEAD_A ** -0.5, "l0_q_pre")
    k = _qkv_pre_fwd(hq, conv, 4, 4, True, 1.0, "l0_k_pre")
    v = _qkv_pre_fwd(hq, conv, 8, 4, False, 1.0, "l0_v_pre")
    gates = _gates_fwd(ba, p['arow'], p['brow'], "l0_gates")
    o, tm_all, s_all = riders.run("l0_gdr_fwd", _gdr_fwd, q, k, v, gates)
    wts['w_glu'], wts['w_out'] = weights.full['w_glu'], weights.full['w_out']
    y_a = _onorm_fwd(o, gate, p['onorm_g'], "l0_onorm")
    bu = riders.run("l0_s5_bu", _mm_bd, u, p['b_in'], "nn")
    xs = riders.run("l0_s5_scan", _s5_scan_fwd, bu, p['a_row'])
    weights.gather_by_sequencer(GATHER_LAYER1, xs, "gather_layer1", GATHER_LAYER1_ID)
    yc = riders.run("l0_s5_cx", _mm_bd, xs, p['c_out'], "nn")
    yl, y_b = _glu_fwd(yc, u, p['d_row'], wts['w_glu'], p['b_glu'], "l0_glu")
    mixed = jnp.concatenate([y_a, y_b], axis=1)
    x1 = _mm(mixed, wts['w_out'], "nn", "l0_out", res=x)
    sv.update(hq=hq, gate=gate, ba=ba, u=u, q=q, k=k, v=v, gb=gates, o=o, tm=tm_all, s=s_all, xs=xs, yl=yl, mixed=mixed)
    return x1, sv


def _hybrid_bwd(dx1, xn, wts, p, sv, riders):
    gr = {}
    dmixed = _mm(dx1, wts['w_out'], "nt", "l0_out_dx", out_dtype=BF16)
    riders.grad('w_out', _as_pieces(_mm(sv['mixed'], dx1, "tn", "l0_out_dw", out_dtype=BF16)))
    dya, dyb = dmixed[:, :WIDTH_A], dmixed[:, WIDTH_A:]
    dyl, du_direct, dw_glu, gr['b_glu_b'], dd = _glu_bwd(
        sv['yl'], sv['u'], p['d_row'], wts['w_glu'], p['b_glu'], dyb, "l0_glu_bwd")
    riders.grad('w_glu', dw_glu.astype(BF16).reshape(N_DEV, -1, PACK_COLS))
    dxs = riders.run("l0_s5_cx_dx", _mm_bd, dyl, p['c_out'], "nt")
    dc_out = _mm_bd(sv['xs'], dyl, "tn", "l0_s5_cx_dw")
    lam, da_row = riders.run("l0_s5_scan_bwd", _s5_scan_bwd, dxs, sv['xs'], p['a_row'])
    du = _mm_bd(lam, p['b_in'], "nt", "l0_s5_bu_dx", res=du_direct, out_dtype=BF16)
    db_in = _mm_bd(sv['u'], lam, "tn", "l0_s5_bu_dw")
    gr['s5'] = (db_in, dc_out, da_row, dd)
    do, dgate, gr['onorm_g_a'] = _onorm_bwd(sv['o'], sv['gate'], p['onorm_g'], dya, "l0_onorm_bwd")
    dq, dk, dv, dgb = riders.run("l0_gdr_bwd", _gdr_bwd, sv['q'], sv['k'], sv['v'], sv['gb'], sv['tm'], sv['s'], do)
    conv = p['conv_qkv']
    dhq_q, dcw_q = _qkv_pre_bwd(sv['hq'], conv, dq, 0, 4, True, HEAD_A ** -0.5, "l0_q_pre_bwd")
    dhq_k, dcw_k = _qkv_pre_bwd(sv['hq'], conv, dk, 4, 4, True, 1.0, "l0_k_pre_bwd")
    dhq_v, dcw_v = _qkv_pre_bwd(sv['hq'], conv, dv, 8, 4, False, 1.0, "l0_v_pre_bwd")
    gr['conv_qkv_a'] = jnp.concatenate([dcw_q, dcw_k, dcw_v], axis=1)
    dhq = jnp.concatenate([dhq_q, dhq_k, dhq_v], axis=1)
    dba, da_log, ddt_bias = _gates_bwd(sv['ba'], p['arow'], p['brow'], dgb, "l0_gates_bwd")
    gr['a_log_a'], gr['dt_bias_a'] = da_log[:, 4:8], ddt_bias[:, 4:8]
    dw_qkv_t = _mm(dhq, xn, "tn", "l0_in_qkv_dw", out_dtype=BF16)
    dw_gate_t = _mm(dgate, xn, "tn", "l0_in_gate_dw", out_dtype=BF16)
    dw_ba_t = _mm(dba, xn, "tn", "l0_in_ba_dw", out_dtype=BF16)
    dw_u_t = _mm(du, xn, "tn", "l0_in_u_dw", out_dtype=BF16)
    dw_in_t = _as_pieces(jnp.concatenate([dw_qkv_t, dw_gate_t, dw_ba_t[:8], dw_u_t], axis=0))
    riders.grad('w_in_t', jnp.concatenate(
        [dw_in_t, jnp.zeros((N_DEV, dict(PIECES)['w_in_t'] - W_IN_PIECE, D_MODEL), BF16)], axis=1))
    dxn = riders.run("l0_in_qkv_dx", _mm, dhq, wts['w_qkv_t'], "nn")
    dxn = _mm(dgate, wts['w_gate_t'], "nn", "l0_in_gate_dx", res=dxn)
    dxn = _mm(dba, wts['w_ba_t'], "nn", "l0_in_ba_dx", res=dxn)
    dxn = riders.run("l0_in_u_dx", _mm, du, wts['w_u_t'], "nn", res=dxn)
    return dxn, gr


def _xa_fwd(x1, g, mem_n, wq, wkv_t, wo, tag, riders):
    xq = _rms_fwd(x1, g, BF16, tag + "_norm")
    q = _mm(xq, wq, "nn", tag + "_q", out_dtype=BF16)
    kv = _mm(mem_n, wkv_t, "nt", tag + "_kv", out_dtype=BF16)
    o = riders.run(tag + "_attn", _attn_fwd, q, kv)
    x2 = _mm(o, wo, "nn", tag + "_o", res=x1)
    return x2, dict(xq=xq, q=q, kv=kv, o=o)


def _xa_bwd(dx2, x1, g, mem_n, wq, wkv_t, wo, sv, tag, layer, riders):
    do = _mm(dx2, wo, "nt", tag + "_o_dx", out_dtype=BF16)
    riders.grad('wo%d' % layer, _as_pieces(_mm(sv['o'], dx2, "tn", tag + "_o_dw", out_dtype=BF16)))
    dq, dk, dv = _attn_bwd(sv['q'], sv['kv'], do, tag + "_attn_bwd")
    dkv = jnp.concatenate([dk, dv], axis=1).astype(BF16)
    dxq = _mm(dq, wq, "nt", tag + "_q_dx")
    riders.grad('wq%d' % layer, _as_pieces(_mm(sv['xq'], dq, "tn", tag + "_q_dw", out_dtype=BF16)))
    dmem_n = _mm(dkv, wkv_t, "nn", tag + "_kv_dx")
    riders.grad('wkv_t%d' % layer, _as_pieces(_mm(dkv, mem_n, "tn", tag + "_kv_dw", out_dtype=BF16)))
    dx1, dg = riders.run(tag + "_norm_bwd", _rms_bwd, x1, g, dxq, dx2)
    return dx1, dmem_n, dg


def _ffn_fwd(x2, g, w_up_t, conv, w_down, tag, riders):
    xf = _rms_fwd(x2, g, BF16, tag + "_norm")
    h = riders.run(tag + "_up", _mm, xf, w_up_t, "nt")
    a = riders.run(tag + "_act", _ffn_act_fwd, h, conv)
    x3 = _mm(a, w_down, "nn", tag + "_down", res=x2)
    return x3, dict(xf=xf, h=h, a=a)


def _ffn_bwd(dx3, x2, g, w_up_t, conv, w_down, sv, tag, layer, riders):
    da = _mm(dx3, w_down, "nt", tag + "_down_dx")
    riders.grad('down%d' % layer, _as_pieces(_mm(sv['a'], dx3, "tn", tag + "_down_dw", out_dtype=BF16)))
    dh, dconv = riders.run(tag + "_act_bwd", _ffn_act_bwd, sv['h'], conv, da)
    dxf = riders.run(tag + "_up_dx", _mm, dh, w_up_t, "nn")
    dw_up_t = riders.run(tag + "_up_dw", _mm, dh, sv['xf'], "tn", out_dtype=BF16)
    riders.grad('up_t%d' % layer, _as_pieces(dw_up_t))
    dx2, dg = riders.run(tag + "_norm_bwd", _rms_bwd, x2, g, dxf, dx3)
    return dx2, dconv, dg


BIG_NAMES, SMALL_NAMES, REP_NAMES = list(BIG_SHARDED), list(SMALL_SHARDED), list(REPLICATED)
BIG_SIZES = [int(np.prod(_shard_shape(*BIG_SHARDED[n]))) for n in BIG_NAMES]
SMALL_SIZES = [int(np.prod(_shard_shape(*SMALL_SHARDED[n]))) for n in SMALL_NAMES]


PIECES = [('w_in_t', 384), ('w_glu', 32), ('w_out', 128), ('pool_w', 32), ('wq0', 128), ('wq1', 128),
          ('wkv_t0', 256), ('wkv_t1', 256), ('wo0', 128), ('wo1', 128), ('up_t0', 704), ('up_t1', 704),
          ('down0', 352), ('down1', 352)]
PIECE_OFFS = dict(zip([k for k, _ in PIECES], np.concatenate([[0], np.cumsum([r for _, r in PIECES])[:-1]]).tolist()))
W_IN_ROWS = 4 * WIDTH_A + 2 * N_HEADS_A + SSM_WIDTH
W_IN_PIECE = W_IN_ROWS // N_DEV


def _row_tile(rows):
    return max(t for t in range(16, min(rows, 512) + 1, 16) if rows % t == 0)


class _Riders:
    def __init__(self):
        self.waiting = {}
        self.deferred = {}
        self.grads = {}
        self.groups = []
        self.reduced = {}

    def add(self, host, comm, then):
        self.waiting.setdefault(host, []).append((comm, then))

    def after(self, marker, then):
        self.deferred.setdefault(marker, []).append(then)

    def mark(self, name, out=None):
        for cont in self.deferred.pop(name, []):
            step = cont()
            if step is not None:
                values, then = step
                out, values = lax.optimization_barrier((out, values))
                then(values)
        return out

    def run(self, name, fn, *args, **kw):
        riders = self.waiting.pop(name, [])
        if not riders:
            out = fn(*args, name=name, **kw)
        else:
            out, couts = fn(*args, name=name, comm=[c for c, _ in riders], **kw)
            for (_, then), got in zip(riders, couts):
                then(got)
        return self.mark(name, out)

    def grad(self, key, pieces):
        self.grads[key] = pieces
        for group in [g for g in self.groups if all(k in self.grads for k in g[1])]:
            self.groups.remove(group)
            self._reduce(*group)

    def _reduce(self, name, keys, pair_marker, sum_marker):
        arrays = [self.grads[k] for k in keys]
        rows = sum(a.shape[1] for a in arrays)
        tile = _row_tile(rows)
        from_sibling = _sequencer_swap(arrays, name + "_to_sibling")

        def after_swap():
            if len(arrays) == 1:
                chip_sums = _pair_sum_pieces(arrays[0], from_sibling, name + "_pair_sum", tr=tile)
            else:
                core = lax.axis_index("c")
                keep = jnp.concatenate(
                    [lax.dynamic_index_in_dim(a.reshape(4, 2, a.shape[1], PACK_COLS), core, 1, keepdims=False)
                     for a in arrays], axis=1)
                chip_sums = _pair_sum(keep, from_sibling, name + "_pair_sum", tr=tile)

            def exchange_among_chips(chip_sums):
                from_chips = _sequencer_chips(chip_sums, name + "_to_chips")

                def store(total):
                    off = 0
                    for k, a in zip(keys, arrays):
                        self.reduced[k] = total[off:off + a.shape[1]]
                        off += a.shape[1]

                self.after(sum_marker, lambda: (
                    _chip_sum(chip_sums, from_chips, name + "_chip_sum", tr=tile), store))

            return chip_sums, exchange_among_chips

        self.after(pair_marker, after_swap)


class _Weights:
    def __init__(self, inp):
        bf = lambda a: a.astype(BF16)
        local = {'w_in_t': bf(inp['w_in_ab'][0]).T, 'w_glu': bf(inp['w_glu_b'][0]), 'w_out': bf(inp['w_out_ab'][0]),
                 'pool_w': bf(inp['pool_w'][0]),
                 'small': _pack([inp[n] for n in SMALL_NAMES])}
        for l in range(2):
            local['wq%d' % l] = bf(inp['xa_wq'][l])
            local['wkv_t%d' % l] = bf(inp['xa_wkv'][l]).T
            local['wo%d' % l] = bf(inp['xa_wo'][l])
            local['up_t%d' % l] = bf(inp['ffn_w_up'][l]).T
            local['down%d' % l] = bf(inp['ffn_w_down'][l])
        self.local, self.full = local, {}

    def plan(self, keys):
        return _gather_comm([self.local[k] for k in keys])

    def gather_by_sequencer(self, keys, after, name, collective_id):
        arrays = [self.local[k] for k in keys]
        tie = (after.reshape(-1)[0] * 0.0).astype(arrays[0].dtype)
        arrays[0] = arrays[0] + tie
        self.land(keys, _sequencer_gather(arrays, name, collective_id))

    def land(self, keys, gathered):
        for k, g in zip(keys, gathered):
            if k == 'small':
                off = 0
                for n, size in zip(SMALL_NAMES, SMALL_SIZES):
                    self.full[n] = _merge_shards(g.reshape(N_DEV, -1)[:, off:off + size], *SMALL_SHARDED[n])
                    off += size
            elif k == 'pool_w':
                self.full[k] = jnp.swapaxes(g, 0, 1).reshape(len(POOL_WINDOWS), POOL_GROUP, POOL_GROUP)
            else:
                self.full[k] = g.reshape(N_DEV * g.shape[1], g.shape[2])


GATHER_FIRST = ['w_in_t', 'small']
GATHER_RIDES = []
GATHER_LAYER0 = ['w_glu', 'w_out', 'wq0', 'wkv_t0', 'wo0', 'down0', 'up_t0']
GATHER_LAYER1 = ['pool_w', 'wq1', 'wkv_t1', 'wo1', 'up_t1', 'down1']
GATHER_LAYER0_ID, GATHER_LAYER1_ID = 7, 8
GRAD_RIDES = [('g_down1', ['down1'], 'l1_ffn_act_bwd', 'l1_xa_norm_bwd'),
              ('g_up1', ['up_t1'], 'l1_xa_norm_bwd', 'l0_ffn_up_dx'),
              ('g_xa1', ['wq1', 'wkv_t1', 'wo1', 'pool_w'], 'l0_ffn_act_bwd', 'l0_xa_norm_bwd'),
              ('g_down0', ['down0'], 'l0_ffn_up_dx', 'l0_s5_scan_bwd'),
              ('g_l0', ['up_t0', 'wq0', 'wkv_t0', 'wo0'], 'l0_s5_cx_dx', 'l0_in_u_dx'),
              ('g_out', ['w_out', 'w_glu'], 'l0_s5_scan_bwd', 'l0_in_u_dx'),
              ('g_in', ['w_in_t'], 'l0_in_u_dx', 'adamw_pool_w')]


def _local_step(inp):
    f32_of = lambda n: inp[n].astype(F32)
    weights = _Weights(inp)
    riders = _Riders()
    riders.groups = list(GRAD_RIDES)
    full = weights.full
    weights.land(GATHER_FIRST, _comm_only(weights.plan(GATHER_FIRST), "gather_first"))
    weights.gather_by_sequencer(GATHER_LAYER0, full['w_in_t'], "gather_layer0", GATHER_LAYER0_ID)
    for host, keys in GATHER_RIDES:
        riders.add(host, weights.plan(keys), functools.partial(weights.land, keys))
    w_in_t = full['w_in_t']
    wts0 = dict(w_qkv_t=w_in_t[:3 * WIDTH_A], w_gate_t=w_in_t[3 * WIDTH_A:4 * WIDTH_A],
                w_ba_t=jnp.concatenate([w_in_t[4 * WIDTH_A:4 * WIDTH_A + 8], jnp.zeros((LANE - 8, D_MODEL), BF16)], 0),
                w_u_t=w_in_t[4 * WIDTH_A + 8:])
    lb_disc, disc_vjp = jax.vjp(_s5_discretise, f32_of('ssm_lambda_re')[0], f32_of('ssm_lambda_im')[0],
                                f32_of('ssm_b_re')[0], f32_of('ssm_b_im')[0], f32_of('ssm_log_dt')[0])
    b_in, c_out, a_row = _s5_matrices(*lb_disc, f32_of('ssm_c_re')[0], f32_of('ssm_c_im')[0])
    zeros4 = jnp.zeros((1, 4), F32)
    p0 = dict(conv_qkv=full['conv_qkv_a'][0], onorm_g=f32_of('onorm_g_a'),
              arow=jnp.concatenate([zeros4, f32_of('a_log_a'), jnp.zeros((1, LANE - 8), F32)], 1),
              brow=jnp.concatenate([zeros4, f32_of('dt_bias_a'), jnp.zeros((1, LANE - 8), F32)], 1),
              b_in=b_in.astype(BF16), c_out=c_out.astype(BF16), a_row=a_row,
              d_row=f32_of('ssm_d').reshape(1, SSM_WIDTH), b_glu=f32_of('b_glu_b'))

    x0 = inp['x'][0]
    mem_n = _rms_fwd(inp['mem'][0], inp['norm_mem_g'], BF16, "mem_norm")
    xn0 = _rms_fwd(x0, inp['norm_mix_g'][0], BF16, "l0_mix_norm")
    x1, sv_mix0 = _hybrid_fwd(xn0, x0, wts0, p0, weights, riders)
    x2, sv_xa0 = _xa_fwd(x1, inp['norm_xa_g'][0], mem_n, full['wq0'], full['wkv_t0'], full['wo0'], "l0_xa", riders)
    x3, sv_ffn0 = _ffn_fwd(x2, inp['norm_ffn_g'][0], full['up_t0'], full['ffn_conv'][0], full['down0'], "l0_ffn", riders)
    xn1 = _rms_fwd(x3, inp['norm_mix_g'][1], F32, "l1_mix_norm")
    x4 = _pool_fwd(xn1, full['pool_w'], full['pool_scale'], x3, "l1_pool")
    x5, sv_xa1 = _xa_fwd(x4, inp['norm_xa_g'][1], mem_n, full['wq1'], full['wkv_t1'], full['wo1'], "l1_xa", riders)
    x6, sv_ffn1 = _ffn_fwd(x5, inp['norm_ffn_g'][1], full['up_t1'], full['ffn_conv'][1], full['down1'], "l1_ffn", riders)
    loss_part, dx6, dg_final = _loss_head(x6, inp['norm_final_g'], inp['loss_target'][0], "loss_head")

    dx5, dconv1, dg_ffn1 = _ffn_bwd(dx6, x5, inp['norm_ffn_g'][1], full['up_t1'], full['ffn_conv'][1], full['down1'],
                                    sv_ffn1, "l1_ffn", 1, riders)
    dx4, dmem1, dg_xa1 = _xa_bwd(dx5, x4, inp['norm_xa_g'][1], mem_n, full['wq1'], full['wkv_t1'], full['wo1'],
                                 sv_xa1, "l1_xa", 1, riders)
    dxn1, dpool_w, dpool_scale = _pool_bwd(xn1, full['pool_w'], full['pool_scale'], dx4, "l1_pool_bwd")
    pool_pieces = jnp.swapaxes(dpool_w.astype(BF16).reshape(len(POOL_WINDOWS), N_DEV, -1, POOL_GROUP), 0, 1)
    riders.grad('pool_w', pool_pieces.reshape(N_DEV, -1, PACK_COLS))
    dx3, dg_mix1 = riders.run("l1_mix_norm_bwd", _rms_bwd, x3, inp['norm_mix_g'][1], dxn1, dx4)
    dx2, dconv0, dg_ffn0 = _ffn_bwd(dx3, x2, inp['norm_ffn_g'][0], full['up_t0'], full['ffn_conv'][0], full['down0'],
                                    sv_ffn0, "l0_ffn", 0, riders)
    dx1, dmem0, dg_xa0 = _xa_bwd(dx2, x1, inp['norm_xa_g'][0], mem_n, full['wq0'], full['wkv_t0'], full['wo0'],
                                 sv_xa0, "l0_xa", 0, riders)
    dxn0, g_mix0 = _hybrid_bwd(dx1, xn0, wts0, p0, sv_mix0, riders)
    grad_x, dg_mix0 = _rms_bwd(x0, inp['norm_mix_g'][0], dxn0, dx1, "l0_mix_norm_bwd")
    _, dg_mem = _rms_bwd(inp['mem'][0], inp['norm_mem_g'], dmem0 + dmem1, None, "mem_norm_bwd")
    assert not riders.groups and not riders.waiting and all(k.startswith("adamw_") for k in riders.deferred), (
        riders.groups, list(riders.waiting), list(riders.deferred))

    db_in, dc_out, da_row, dd = g_mix0['s5']
    dlb_re, dlb_im, dbb_re, dbb_im, dc_re, dc_im = _s5_matrix_grads(db_in, dc_out, da_row)
    dlam_re, dlam_im, dbr, dbi, dlog_dt = disc_vjp((dlb_re, dlb_im, dbb_re, dbb_im))

    rep_grads = {
        'norm_mix_g': jnp.concatenate([dg_mix0, dg_mix1], 0), 'norm_xa_g': jnp.concatenate([dg_xa0, dg_xa1], 0),
        'norm_ffn_g': jnp.concatenate([dg_ffn0, dg_ffn1], 0), 'norm_mem_g': dg_mem.reshape(-1),
        'norm_final_g': dg_final.reshape(-1), 'a_log_a': g_mix0['a_log_a'], 'dt_bias_a': g_mix0['dt_bias_a'],
        'onorm_g_a': g_mix0['onorm_g_a'], 'ssm_lambda_re': dlam_re[None], 'ssm_lambda_im': dlam_im[None],
        'ssm_b_re': dbr[None], 'ssm_b_im': dbi[None], 'ssm_c_re': dc_re[None], 'ssm_c_im': dc_im[None],
        'ssm_d': dd.reshape(1, N_GROUPS, SSM_GROUP), 'ssm_log_dt': dlog_dt[None], 'b_glu_b': g_mix0['b_glu_b']}
    small_grads = {'conv_qkv_a': g_mix0['conv_qkv_a'][None], 'pool_scale': dpool_scale,
                   'ffn_conv': jnp.stack([dconv0, dconv1])}
    return loss_part, grad_x, riders, rep_grads, small_grads


ADAMW_ORDER = ['ffn_w_up', 'ffn_w_down', 'xa_wkv', 'xa_wq', 'xa_wo', 'w_out_ab', 'w_glu_b', 'pool_w', 'w_in_ab']


def _update(inp, loss_part, grad_x, riders, rep_grads, small_grads):
    dev = _device_index()
    misc_local = _pack([rep_grads[n] for n in REP_NAMES] + [small_grads[n] for n in SMALL_NAMES] + [loss_part])
    (misc_all,) = _sequencer_gather([misc_local], "gather_small_grads", GATHER_LAYER0_ID)
    piece = lambda key: riders.reduced[key]
    both = lambda name: jnp.stack([piece(name + '0'), piece(name + '1')])
    swap = lambda a: jnp.swapaxes(a, -1, -2)
    reduced = {'w_in_ab': lambda: piece('w_in_t')[:W_IN_PIECE][None],
               'w_glu_b': lambda: piece('w_glu').reshape(inp['w_glu_b'].shape),
               'w_out_ab': lambda: piece('w_out')[None], 'pool_w': lambda: piece('pool_w').reshape(inp['pool_w'].shape),
               'xa_wq': lambda: both('wq'), 'xa_wkv': lambda: both('wkv_t'), 'xa_wo': lambda: both('wo'),
               'ffn_w_up': lambda: both('up_t'), 'ffn_w_down': lambda: both('down')}
    transposed = ('w_in_ab', 'xa_wkv', 'ffn_w_up')
    grads, upd = {}, {}
    assert sorted(ADAMW_ORDER) == sorted(BIG_NAMES)
    for n in ADAMW_ORDER:
        fix = swap if n in transposed else (lambda a: a)
        g = reduced[n]()
        out = riders.run("adamw_" + n, _adamw, fix(inp[n]), g, fix(inp['m_' + n]), fix(inp['v_' + n]))
        upd[n], grads[n] = tuple(fix(o) for o in out), fix(g)
    assert not riders.waiting and not riders.deferred, (list(riders.waiting), list(riders.deferred))
    misc_sum = _sum_leading(misc_all, "small_grads_sum")
    misc = _unpack(misc_sum, [inp[n].shape for n in REP_NAMES] + [SMALL_SHARDED[n][0] for n in SMALL_NAMES] + [()])
    loss = misc.pop()
    for n, g in zip(REP_NAMES, misc):
        grads[n] = g
    for n, g in zip(SMALL_NAMES, misc[len(REP_NAMES):]):
        grads[n] = lax.dynamic_index_in_dim(_split_shards(g, SMALL_SHARDED[n][1]), dev, 0, keepdims=False
                                            ).reshape(inp[n].shape)
    tiny_names = REP_NAMES + SMALL_NAMES
    rep_total = sum(int(np.prod(inp[n].shape)) for n in REP_NAMES)
    packs = [_pack([inp[prefix + n] for n in tiny_names]) for prefix in ('', 'm_', 'v_')]
    g_pack = _pack([misc_sum.reshape(-1)[:rep_total]] + [grads[n] for n in SMALL_NAMES])
    tiny_out = [_unpack(o, [inp[n].shape for n in tiny_names])
                for o in _adamw(packs[0], g_pack, packs[1], packs[2], "adamw_small")]
    for i, n in enumerate(tiny_names):
        upd[n] = tuple(o[i] for o in tiny_out)

    outs = [loss, grad_x[None]]
    outs += [grads[n] for n in WEIGHT_NAMES]
    for i in range(3):
        outs += [upd[n][i] for n in WEIGHT_NAMES]
    return tuple(outs)


def _step(inp):
    loss_part, grad_x, riders, rep_grads, small_grads = _local_step(inp)
    return _update(inp, loss_part, grad_x, riders, rep_grads, small_grads)


INPUT_NAMES = (['x', 'mem'] + WEIGHT_NAMES + ['loss_target'] + ['m_' + n for n in WEIGHT_NAMES]
               + ['v_' + n for n in WEIGHT_NAMES])


def kernel(x, mem, norm_mix_g, norm_xa_g, norm_ffn_g, norm_mem_g, norm_final_g, w_in_ab, conv_qkv_a, a_log_a, dt_bias_a, onorm_g_a, ssm_lambda_re, ssm_lambda_im, ssm_b_re, ssm_b_im, ssm_c_re, ssm_c_im, ssm_d, ssm_log_dt, w_glu_b, b_glu_b, w_out_ab, pool_w, pool_scale, xa_wq, xa_wkv, xa_wo, ffn_w_up, ffn_conv, ffn_w_down, loss_target, m_norm_mix_g, m_norm_xa_g, m_norm_ffn_g, m_norm_mem_g, m_norm_final_g, m_w_in_ab, m_conv_qkv_a, m_a_log_a, m_dt_bias_a, m_onorm_g_a, m_ssm_lambda_re, m_ssm_lambda_im, m_ssm_b_re, m_ssm_b_im, m_ssm_c_re, m_ssm_c_im, m_ssm_d, m_ssm_log_dt, m_w_glu_b, m_b_glu_b, m_w_out_ab, m_pool_w, m_pool_scale, m_xa_wq, m_xa_wkv, m_xa_wo, m_ffn_w_up, m_ffn_conv, m_ffn_w_down, v_norm_mix_g, v_norm_xa_g, v_norm_ffn_g, v_norm_mem_g, v_norm_final_g, v_w_in_ab, v_conv_qkv_a, v_a_log_a, v_dt_bias_a, v_onorm_g_a, v_ssm_lambda_re, v_ssm_lambda_im, v_ssm_b_re, v_ssm_b_im, v_ssm_c_re, v_ssm_c_im, v_ssm_d, v_ssm_log_dt, v_w_glu_b, v_b_glu_b, v_w_out_ab, v_pool_w, v_pool_scale, v_xa_wq, v_xa_wkv, v_xa_wo, v_ffn_w_up, v_ffn_conv, v_ffn_w_down):
    args = (x, mem, norm_mix_g, norm_xa_g, norm_ffn_g, norm_mem_g, norm_final_g, w_in_ab, conv_qkv_a, a_log_a, dt_bias_a, onorm_g_a, ssm_lambda_re, ssm_lambda_im, ssm_b_re, ssm_b_im, ssm_c_re, ssm_c_im, ssm_d, ssm_log_dt, w_glu_b, b_glu_b, w_out_ab, pool_w, pool_scale, xa_wq, xa_wkv, xa_wo, ffn_w_up, ffn_conv, ffn_w_down, loss_target, m_norm_mix_g, m_norm_xa_g, m_norm_ffn_g, m_norm_mem_g, m_norm_final_g, m_w_in_ab, m_conv_qkv_a, m_a_log_a, m_dt_bias_a, m_onorm_g_a, m_ssm_lambda_re, m_ssm_lambda_im, m_ssm_b_re, m_ssm_b_im, m_ssm_c_re, m_ssm_c_im, m_ssm_d, m_ssm_log_dt, m_w_glu_b, m_b_glu_b, m_w_out_ab, m_pool_w, m_pool_scale, m_xa_wq, m_xa_wkv, m_xa_wo, m_ffn_w_up, m_ffn_conv, m_ffn_w_down, v_norm_mix_g, v_norm_xa_g, v_norm_ffn_g, v_norm_mem_g, v_norm_final_g, v_w_in_ab, v_conv_qkv_a, v_a_log_a, v_dt_bias_a, v_onorm_g_a, v_ssm_lambda_re, v_ssm_lambda_im, v_ssm_b_re, v_ssm_b_im, v_ssm_c_re, v_ssm_c_im, v_ssm_d, v_ssm_log_dt, v_w_glu_b, v_b_glu_b, v_w_out_ab, v_pool_w, v_pool_scale, v_xa_wq, v_xa_wkv, v_xa_wo, v_ffn_w_up, v_ffn_conv, v_ffn_w_down)
    return _step(dict(zip(INPUT_NAMES, args)))
```

```python
import functools
import math

import numpy as np
import jax
import jax.numpy as jnp
from jax import lax
from jax.experimental import pallas as pl
from jax.experimental.pallas import tpu as pltpu
from jax.experimental.pallas import tpu_sc as plsc

F32, BF16 = jnp.float32, jnp.bfloat16
HIGH, HIGHEST = lax.Precision.HIGH, lax.Precision.HIGHEST
MESH = pl.DeviceIdType.MESH

N_DEV = 8
SEQ, D_MODEL, MEM_LEN = 2048, 1024, 256
WIDTH_A, N_HEADS_A, HEAD_A, CONV_A = 512, 4, 128, 4
GDR_CHUNK = 128
SSM_WIDTH, SSM_GROUP, N_GROUPS, SSM_STATE = 512, 16, 32, 64
SSM_CH = N_GROUPS * SSM_STATE
SCAN_CB = 512
POOL_WINDOWS = (2, 4, 8, 16)
POOL_GROUP = 256
N_HEADS_X, HEAD_X = 4, 256
D_FF, CONV_FFN = 2816, 3
RMS_EPS = 1e-6
ADAM_LR, ADAM_B1, ADAM_B2, ADAM_EPS, ADAM_WD, ADAM_STEP = 0.001, 0.9, 0.999, 1e-08, 0.01, 10
LANE = 128
PACK_COLS = 1024
VMEM_LIMIT_BYTES = 56 * 1024 * 1024


def _params(sem=None):
    return pltpu.CompilerParams(dimension_semantics=sem, vmem_limit_bytes=VMEM_LIMIT_BYTES)


class Comm:
    def __init__(self, inputs, out_shapes, sems, start, end, mid=None):
        self.inputs, self.out_shapes, self.sems = list(inputs), list(out_shapes), list(sems)
        self.start, self.mid, self.end = start, mid, end


def _merge_comms(comms):
    comms = [c for c in comms if c is not None]
    if not comms:
        return None, []
    bounds, ni, no, ns = [], 0, 0, 0
    for c in comms:
        bounds.append((ni, no, ns))
        ni, no, ns = ni + len(c.inputs), no + len(c.out_shapes), ns + len(c.sems)

    def phase(which):
        def run(ins, outs, sems):
            for c, (i0, o0, s0) in zip(comms, bounds):
                fn = getattr(c, which)
                if fn is not None:
                    fn(ins[i0:i0 + len(c.inputs)], outs[o0:o0 + len(c.out_shapes)], sems[s0:s0 + len(c.sems)])
        return run

    merged = Comm([a for c in comms for a in c.inputs], [s for c in comms for s in c.out_shapes],
                  [s for c in comms for s in c.sems], phase("start"), phase("end"), phase("mid"))
    return merged, [(o0, o0 + len(c.out_shapes)) for c, (_, o0, _) in zip(comms, bounds)]


def _call(body, *, name, grid, in_specs, out_specs, out_shape, args, scratch_shapes=(), sem=None, comm=None,
          aliases=None):
    single = not isinstance(out_shape, (list, tuple))
    out_specs_l = [out_specs] if single else list(out_specs)
    out_shape_l = [out_shape] if single else list(out_shape)
    scratch_shapes = list(scratch_shapes)
    merged, spans = _merge_comms(comm if isinstance(comm, (list, tuple)) else [comm])
    assert merged is None or not aliases
    if merged is None:
        outs = pl.pallas_call(body, name=name, grid=grid, in_specs=list(in_specs), out_specs=out_specs_l,
                              out_shape=out_shape_l, scratch_shapes=scratch_shapes,
                              input_output_aliases=dict(aliases or {}), compiler_params=_params(sem))(*args)
        outs = outs[0] if single else outs
        return outs if comm is None else (outs, [])
    n_in, n_out, n_scr = len(in_specs), len(out_specs_l), len(scratch_shapes)
    ci, co = len(merged.inputs), len(merged.out_shapes)
    total = int(np.prod(grid))

    def wrapped(*refs):
        ins, cins = refs[:n_in], refs[n_in:n_in + ci]
        outs, couts = refs[n_in + ci:n_in + ci + n_out], refs[n_in + ci + n_out:n_in + ci + n_out + co]
        scr, csems = refs[n_in + ci + n_out + co:n_in + ci + n_out + co + n_scr], refs[n_in + ci + n_out + co + n_scr:]
        lin = pl.program_id(0)
        for d in range(1, len(grid)):
            lin = lin * grid[d] + pl.program_id(d)
        pl.when(lin == 0)(lambda: merged.start(cins, couts, csems))
        body(*ins, *outs, *scr)
        mid_step = min((3 * total) // 4, total - 1)
        pl.when(lin == mid_step)(lambda: merged.mid(cins, couts, csems))
        pl.when(lin == total - 1)(lambda: merged.end(cins, couts, csems))

    any_spec = pl.BlockSpec(memory_space=pl.ANY)
    res = pl.pallas_call(
        wrapped, name=name, grid=grid, in_specs=list(in_specs) + [any_spec] * ci,
        out_specs=out_specs_l + [any_spec] * co, out_shape=out_shape_l + merged.out_shapes,
        scratch_shapes=scratch_shapes + merged.sems,
        compiler_params=_params(("arbitrary",) * len(grid)))(*args, *merged.inputs)
    outs, couts = res[:n_out], res[n_out:]
    return (outs[0] if single else list(outs)), [list(couts[a:b]) for a, b in spans]


def _comm_only(comm, name):
    def body():
        pass

    _, couts = _call(body, name=name, grid=(1,), in_specs=[], out_specs=[], out_shape=[], args=[], comm=comm)
    return couts[0]


def _tile(dim, pref):
    best = None
    for t in range(LANE, min(dim, pref) + 1, LANE):
        if dim % t == 0:
            best = t
    return best if best is not None else dim


MM_VMEM_BUDGET = 40 * 1024 * 1024


def _mm_tiles(m, n, k, a_bytes, b_bytes, o_bytes, r_bytes):
    for tk in (k, _tile(k, 2048), _tile(k, 1024), _tile(k, 512)):
        for tm, tn in ((1024, 1536), (1024, 1024), (1024, 512), (512, 512), (256, 512), (256, 256)):
            tm, tn = _tile(m, tm), _tile(n, tn)
            acc = 0 if tk == k else tm * tn * 4
            need = 2 * (tm * tk * a_bytes + tk * tn * b_bytes + tm * tn * (o_bytes + r_bytes)) + acc
            if need <= MM_VMEM_BUDGET:
                return tm, tn, tk
    raise ValueError("no matmul tiling fits VMEM")


def _mm(a, b, mode, name, out_dtype=F32, res=None, comm=None, b_k_part=0, into=None):
    if mode == "nn":
        (m, k), n = a.shape, b.shape[1]
    elif mode == "nt":
        (m, k), n = a.shape, b.shape[0]
    else:
        (k, m), n = a.shape, b.shape[1]
    tm, tn, tk = _mm_tiles(m, n, k, a.dtype.itemsize, b.dtype.itemsize, jnp.dtype(out_dtype).itemsize,
                           0 if res is None else res.dtype.itemsize)
    nk = k // tk
    dims = {"nn": ((1,), (0,)), "nt": ((1,), (1,)), "tn": ((0,), (0,))}[mode]

    def body(*refs):
        if res is None:
            a_ref, b_ref, o_ref = refs[:3]
            r_ref = None
        else:
            a_ref, b_ref, r_ref, o_ref = refs[:4]
        part = lax.dot_general(a_ref[...].astype(BF16), b_ref[...].astype(BF16), (dims, ((), ())),
                               preferred_element_type=F32)

        def finish(out):
            if r_ref is not None:
                out = out + r_ref[...].astype(F32)
            o_ref[...] = out.astype(out_dtype)

        if nk == 1:
            finish(part)
            return
        acc = refs[-1]
        kk = pl.program_id(2)

        @pl.when(kk == 0)
        def _():
            acc[...] = part

        @pl.when(kk > 0)
        def _():
            acc[...] += part

        @pl.when(kk == nk - 1)
        def _():
            finish(acc[...])

    k_off = b_k_part * nk
    a_spec = (pl.BlockSpec((tk, tm), lambda i, j, q: (q, i)) if mode == "tn"
              else pl.BlockSpec((tm, tk), lambda i, j, q: (i, q)))
    b_spec = (pl.BlockSpec((tn, tk), lambda i, j, q: (j, q)) if mode == "nt"
              else pl.BlockSpec((tk, tn), lambda i, j, q: (q + k_off, j)))
    res_spec = pl.BlockSpec((tm, tn), lambda i, j, q: (i, j))
    out_rows, row_off, aliases = m, 0, None
    in_specs, args = [a_spec, b_spec], [a, b]
    if res is not None:
        in_specs.append(res_spec)
        args.append(res)
    if into is not None:
        out_rows, row_off = into[0], into[1] * (m // tm)
        if len(into) > 2:
            aliases = {len(args): 0}
            in_specs.append(pl.BlockSpec(memory_space=pl.ANY))
            args.append(into[2])
    o_spec = pl.BlockSpec((tm, tn), lambda i, j, q: (i + row_off, j))

    def body_io(*refs):
        n_in = len(args)
        body(*refs[:2 + (res is not None)], *refs[n_in:])

    return _call(body_io, name=name, grid=(m // tm, n // tn, nk), in_specs=in_specs, out_specs=o_spec,
                 out_shape=jax.ShapeDtypeStruct((out_rows, n), out_dtype),
                 scratch_shapes=[] if nk == 1 else [pltpu.VMEM((tm, tn), F32)],
                 sem=("parallel", "parallel", "arbitrary"), args=args, comm=comm, aliases=aliases)


def _mm_bd(a, b, mode, name, out_dtype=F32, res=None, comm=None, tm=1024):
    if mode == "tn":
        k = a.shape[0]
        nb = min(a.shape[1], b.shape[1]) // LANE
        ma, n = a.shape[1] // nb, b.shape[1] // nb

        def body(a_ref, b_ref, o_ref):
            o_ref[0] = lax.dot_general(a_ref[...].astype(BF16), b_ref[...].astype(BF16), (((0,), (0,)), ((), ())),
                                       preferred_element_type=F32).astype(out_dtype)

        return _call(body, name=name, grid=(nb,),
                     in_specs=[pl.BlockSpec((k, ma), lambda j: (0, j)), pl.BlockSpec((k, n), lambda j: (0, j))],
                     out_specs=pl.BlockSpec((1, ma, n), lambda j: (j, 0, 0)),
                     out_shape=jax.ShapeDtypeStruct((nb, ma, n), out_dtype), sem=("parallel",), args=(a, b), comm=comm)
    m = a.shape[0]
    nb = b.shape[0]
    ka = a.shape[1] // nb
    n = b.shape[2] if mode == "nn" else b.shape[1]
    tm = _tile(m, tm)
    dims = ((1,), (0,)) if mode == "nn" else ((1,), (1,))

    def body(*refs):
        if res is None:
            a_ref, b_ref, o_ref = refs
            r_ref = None
        else:
            a_ref, b_ref, r_ref, o_ref = refs
        out = lax.dot_general(a_ref[...].astype(BF16), b_ref[0].astype(BF16), (dims, ((), ())),
                              preferred_element_type=F32)
        if r_ref is not None:
            out = out + r_ref[...].astype(F32)
        o_ref[...] = out.astype(out_dtype)

    o_spec = pl.BlockSpec((tm, n), lambda i, j: (i, j))
    in_specs = [pl.BlockSpec((tm, ka), lambda i, j: (i, j)), pl.BlockSpec((1,) + b.shape[1:], lambda i, j: (j, 0, 0))]
    args = [a, b]
    if res is not None:
        in_specs.append(o_spec)
        args.append(res)
    return _call(body, name=name, grid=(m // tm, nb), in_specs=in_specs, out_specs=o_spec,
                 out_shape=jax.ShapeDtypeStruct((m, nb * n), out_dtype), sem=("parallel", "parallel"),
                 args=args, comm=comm)


def _rms_fwd(x, g, out_dtype, name, tr=256):
    rows, d = x.shape

    def body(x_ref, g_ref, o_ref):
        xv = x_ref[...]
        r = lax.rsqrt(jnp.mean(xv * xv, axis=-1, keepdims=True) + RMS_EPS)
        o_ref[...] = (xv * r * g_ref[...]).astype(out_dtype)

    return pl.pallas_call(
        body, name=name, grid=(rows // tr,),
        in_specs=[pl.BlockSpec((tr, d), lambda i: (i, 0)), pl.BlockSpec((1, d), lambda i: (0, 0))],
        out_specs=pl.BlockSpec((tr, d), lambda i: (i, 0)), out_shape=jax.ShapeDtypeStruct((rows, d), out_dtype),
        compiler_params=_params(("parallel",)))(x, g.reshape(1, d))


def _rms_bwd(x, g, dy, dres, name, tr=256, comm=None):
    rows, d = x.shape

    def body(*refs):
        if dres is None:
            x_ref, g_ref, dy_ref, dx_ref, dg_ref = refs
            r_ref = None
        else:
            x_ref, g_ref, dy_ref, r_ref, dx_ref, dg_ref = refs

        @pl.when(pl.program_id(0) == 0)
        def _():
            dg_ref[...] = jnp.zeros_like(dg_ref)

        xv, dyv = x_ref[...], dy_ref[...].astype(F32)
        r = lax.rsqrt(jnp.mean(xv * xv, axis=-1, keepdims=True) + RMS_EPS)
        xh = xv * r
        dyg = dyv * g_ref[...]
        dx = r * (dyg - xh * jnp.mean(dyg * xh, axis=-1, keepdims=True))
        if r_ref is not None:
            dx = dx + r_ref[...]
        dx_ref[...] = dx
        dg_ref[...] += jnp.sum(dyv * xh, axis=0, keepdims=True)

    blk = pl.BlockSpec((tr, d), lambda i: (i, 0))
    vec = pl.BlockSpec((1, d), lambda i: (0, 0))
    in_specs, args = [blk, vec, blk], [x, g.reshape(1, d), dy]
    if dres is not None:
        in_specs.append(blk)
        args.append(dres)
    return _call(
        body, name=name, grid=(rows // tr,), in_specs=in_specs, out_specs=[blk, vec],
        out_shape=[jax.ShapeDtypeStruct((rows, d), F32), jax.ShapeDtypeStruct((1, d), F32)],
        sem=("arbitrary",), args=args, comm=comm)


def _loss_head(x, g, target, name, tr=256):
    rows, d = x.shape

    def body(x_ref, g_ref, t_ref, loss_ref, dx_ref, dg_ref):
        @pl.when(pl.program_id(0) == 0)
        def _():
            dg_ref[...] = jnp.zeros_like(dg_ref)
            loss_ref[...] = jnp.zeros_like(loss_ref)

        xv = x_ref[...]
        r = lax.rsqrt(jnp.mean(xv * xv, axis=-1, keepdims=True) + RMS_EPS)
        xh = xv * r
        err = xh * g_ref[...] - t_ref[...]
        loss_ref[...] += 0.5 * jnp.sum(jnp.mean(err * err, axis=-1, keepdims=True), keepdims=True)
        dyv = err * (1.0 / d)
        dyg = dyv * g_ref[...]
        dx_ref[...] = r * (dyg - xh * jnp.mean(dyg * xh, axis=-1, keepdims=True))
        dg_ref[...] += jnp.sum(dyv * xh, axis=0, keepdims=True)

    blk = pl.BlockSpec((tr, d), lambda i: (i, 0))
    vec = pl.BlockSpec((1, d), lambda i: (0, 0))
    return pl.pallas_call(
        body, name=name, grid=(rows // tr,), in_specs=[blk, vec, blk],
        out_specs=[pl.BlockSpec((1, 1), lambda i: (0, 0)), blk, vec],
        out_shape=[jax.ShapeDtypeStruct((1, 1), F32), jax.ShapeDtypeStruct((rows, d), F32),
                   jax.ShapeDtypeStruct((1, d), F32)],
        compiler_params=_params(("arbitrary",)))(x, g.reshape(1, d), target)


def _shift_down(x, s):
    rows = lax.broadcasted_iota(jnp.int32, x.shape, 0)
    return jnp.where(rows >= s, pltpu.roll(x, s, 0), 0.0)


def _shift_up(x, s):
    n = x.shape[0]
    rows = lax.broadcasted_iota(jnp.int32, x.shape, 0)
    return jnp.where(rows < n - s, pltpu.roll(x, n - s, 0), 0.0)


def _sigmoid(x):
    return 1.0 / (1.0 + jnp.exp(-x))


def _silu_and_grad(x):
    s = _sigmoid(x)
    return x * s, s * (1.0 + x * (1.0 - s))


_GELU_C0, _GELU_C1 = math.sqrt(2.0 / math.pi), 0.044715


def _gelu_and_grad(x):
    th = jnp.tanh(_GELU_C0 * (x + _GELU_C1 * x * x * x))
    y = 0.5 * x * (1.0 + th)
    dy = 0.5 * (1.0 + th) + 0.5 * x * (1.0 - th * th) * _GELU_C0 * (1.0 + 3.0 * _GELU_C1 * x * x)
    return y, dy


def _ffn_act_fwd(h, w, name, tc=256, comm=None):
    t = h.shape[0]
    nb = D_FF // tc

    def body(hg_ref, hv_ref, wg_ref, wv_ref, a_ref):
        def conv(x, wr):
            return wr[2:3, :] * x + wr[1:2, :] * _shift_down(x, 1) + wr[0:1, :] * _shift_down(x, 2)

        cg = conv(hg_ref[...], wg_ref[...])
        cv = conv(hv_ref[...], wv_ref[...])
        a_ref[...] = (cg * _sigmoid(cg) * cv).astype(BF16)

    return _call(
        body, name=name, grid=(nb,),
        in_specs=[pl.BlockSpec((t, tc), lambda j: (0, j)), pl.BlockSpec((t, tc), lambda j: (0, j + nb)),
                  pl.BlockSpec((CONV_FFN, tc), lambda j: (0, j)), pl.BlockSpec((CONV_FFN, tc), lambda j: (0, j + nb))],
        out_specs=pl.BlockSpec((t, tc), lambda j: (0, j)), out_shape=jax.ShapeDtypeStruct((t, D_FF), BF16),
        sem=("parallel",), args=(h, h, w, w), comm=comm)


def _ffn_act_bwd(h, w, da, name, tc=256, comm=None):
    t = h.shape[0]
    nb = D_FF // tc

    def body(hg_ref, hv_ref, wg_ref, wv_ref, da_ref, dhg_ref, dhv_ref, dwg_ref, dwv_ref):
        hg, hv, wg, wv = hg_ref[...], hv_ref[...], wg_ref[...], wv_ref[...]
        hg1, hg2, hv1, hv2 = _shift_down(hg, 1), _shift_down(hg, 2), _shift_down(hv, 1), _shift_down(hv, 2)
        cg = wg[2:3, :] * hg + wg[1:2, :] * hg1 + wg[0:1, :] * hg2
        cv = wv[2:3, :] * hv + wv[1:2, :] * hv1 + wv[0:1, :] * hv2
        sg, dsg = _silu_and_grad(cg)
        dav = da_ref[...].astype(F32)
        dcv = dav * sg
        dcg = dav * cv * dsg

        def conv_t(dc, wr):
            return wr[2:3, :] * dc + wr[1:2, :] * _shift_up(dc, 1) + wr[0:1, :] * _shift_up(dc, 2)

        dhg_ref[...] = conv_t(dcg, wg).astype(BF16)
        dhv_ref[...] = conv_t(dcv, wv).astype(BF16)
        dwg_ref[0:1, :] = jnp.sum(dcg * hg2, axis=0, keepdims=True)
        dwg_ref[1:2, :] = jnp.sum(dcg * hg1, axis=0, keepdims=True)
        dwg_ref[2:3, :] = jnp.sum(dcg * hg, axis=0, keepdims=True)
        dwv_ref[0:1, :] = jnp.sum(dcv * hv2, axis=0, keepdims=True)
        dwv_ref[1:2, :] = jnp.sum(dcv * hv1, axis=0, keepdims=True)
        dwv_ref[2:3, :] = jnp.sum(dcv * hv, axis=0, keepdims=True)

    big = lambda off: pl.BlockSpec((t, tc), lambda j: (0, j + off))
    small = lambda off: pl.BlockSpec((CONV_FFN, tc), lambda j: (0, j + off))
    res = _call(
        body, name=name, grid=(nb,),
        in_specs=[big(0), big(nb), small(0), small(nb), big(0)],
        out_specs=[big(0), big(0), small(0), small(0)],
        out_shape=[jax.ShapeDtypeStruct((t, D_FF), BF16), jax.ShapeDtypeStruct((t, D_FF), BF16),
                   jax.ShapeDtypeStruct((CONV_FFN, D_FF), F32), jax.ShapeDtypeStruct((CONV_FFN, D_FF), F32)],
        sem=("parallel",), args=(h, h, w, w, da), comm=comm)
    (dhg, dhv, dwg, dwv), couts = res if comm is not None else (res, None)
    out = (dhg, dhv, jnp.concatenate([dwg, dwv], axis=1))
    return out if comm is None else (out, couts)


def _attn_probs(q, k):
    s = lax.dot_general(q.astype(BF16), k.astype(BF16), (((1,), (1,)), ((), ())),
                        preferred_element_type=F32) * (HEAD_X ** -0.5)
    s = s - jnp.max(s, axis=-1, keepdims=True)
    p = jnp.exp(s)
    return p / jnp.sum(p, axis=-1, keepdims=True)


def _attn_fwd(q, kv, name, tq=512, comm=None):
    t = q.shape[0]

    def body(q_ref, k_ref, v_ref, o_ref):
        p = _attn_probs(q_ref[...], k_ref[...])
        o_ref[...] = jnp.dot(p.astype(BF16), v_ref[...].astype(BF16), preferred_element_type=F32).astype(BF16)

    return _call(
        body, name=name, grid=(N_HEADS_X, t // tq),
        in_specs=[pl.BlockSpec((tq, HEAD_X), lambda h, i: (i, h)),
                  pl.BlockSpec((MEM_LEN, HEAD_X), lambda h, i: (0, h)),
                  pl.BlockSpec((MEM_LEN, HEAD_X), lambda h, i: (0, h + N_HEADS_X))],
        out_specs=pl.BlockSpec((tq, HEAD_X), lambda h, i: (i, h)),
        out_shape=jax.ShapeDtypeStruct((t, N_HEADS_X * HEAD_X), BF16),
        sem=("parallel", "parallel"), args=(q, kv, kv), comm=comm)


def _attn_bwd(q, kv, do, name, tq=512):
    t = q.shape[0]

    def body(q_ref, k_ref, v_ref, do_ref, dq_ref, dk_ref, dv_ref):
        @pl.when(pl.program_id(1) == 0)
        def _():
            dk_ref[...] = jnp.zeros_like(dk_ref)
            dv_ref[...] = jnp.zeros_like(dv_ref)

        qb, kb, vb, dob = (r[...].astype(BF16) for r in (q_ref, k_ref, v_ref, do_ref))
        p = _attn_probs(qb, kb)
        dp = lax.dot_general(dob, vb, (((1,), (1,)), ((), ())), preferred_element_type=F32)
        ds = p * (dp - jnp.sum(dp * p, axis=-1, keepdims=True)) * (HEAD_X ** -0.5)
        dsb = ds.astype(BF16)
        dq_ref[...] = jnp.dot(dsb, kb, preferred_element_type=F32).astype(BF16)
        dk_ref[...] += lax.dot_general(dsb, qb, (((0,), (0,)), ((), ())), preferred_element_type=F32)
        dv_ref[...] += lax.dot_general(p.astype(BF16), dob, (((0,), (0,)), ((), ())), preferred_element_type=F32)

    qs = pl.BlockSpec((tq, HEAD_X), lambda h, i: (i, h))
    ms = pl.BlockSpec((MEM_LEN, HEAD_X), lambda h, i: (0, h))
    return pl.pallas_call(
        body, name=name, grid=(N_HEADS_X, t // tq),
        in_specs=[qs, ms, pl.BlockSpec((MEM_LEN, HEAD_X), lambda h, i: (0, h + N_HEADS_X)), qs],
        out_specs=[qs, ms, ms],
        out_shape=[jax.ShapeDtypeStruct((t, D_MODEL), BF16), jax.ShapeDtypeStruct((MEM_LEN, D_MODEL), F32),
                   jax.ShapeDtypeStruct((MEM_LEN, D_MODEL), F32)],
        compiler_params=_params(("parallel", "arbitrary")))(q, kv, kv, do)


def _pool_counts(t, win):
    pos = lax.broadcasted_iota(jnp.int32, (t, 1), 0).astype(F32) + 1.0
    return 1.0 / jnp.minimum(pos, float(win))


def _pool_delta(xv, win):
    s, step = xv, 1
    while step < win:
        s = s + _shift_down(s, step)
        step *= 2
    return s * _pool_counts(xv.shape[0], win) - xv


def _pool_delta_t(dv, win):
    s, step = dv * _pool_counts(dv.shape[0], win), 1
    while step < win:
        s = s + _shift_up(s, step)
        step *= 2
    return s - dv


def _pool_fwd(xn, w, scale, res, name):
    t = xn.shape[0]

    def make_branch(win, xn_ref, w_ref, s_ref, r_ref, o_ref):
        def branch():
            dl = _pool_delta(xn_ref[...], win)
            y = jnp.dot(dl.astype(BF16), w_ref[0], preferred_element_type=F32)
            o_ref[...] = r_ref[...] + y * s_ref[...]
        return branch

    def body(xn_ref, w_ref, s_ref, r_ref, o_ref):
        for gi, win in enumerate(POOL_WINDOWS):
            pl.when(pl.program_id(0) == gi)(make_branch(win, xn_ref, w_ref, s_ref, r_ref, o_ref))

    blk = pl.BlockSpec((t, POOL_GROUP), lambda g: (0, g))
    return pl.pallas_call(
        body, name=name, grid=(len(POOL_WINDOWS),),
        in_specs=[blk, pl.BlockSpec((1, POOL_GROUP, POOL_GROUP), lambda g: (g, 0, 0)),
                  pl.BlockSpec((1, POOL_GROUP), lambda g: (0, g)), blk],
        out_specs=blk, out_shape=jax.ShapeDtypeStruct((t, D_MODEL), F32),
        compiler_params=_params(("parallel",)))(xn, w, scale, res)


def _pool_bwd(xn, w, scale, dmix, name):
    t = xn.shape[0]

    def make_branch(win, xn_ref, w_ref, s_ref, d_ref, dxn_ref, dw_ref, ds_ref):
        def branch():
            dl = _pool_delta(xn_ref[...], win).astype(BF16)
            wv = w_ref[0]
            dm = d_ref[...]
            y = jnp.dot(dl, wv, preferred_element_type=F32)
            ds_ref[...] = jnp.sum(dm * y, axis=0, keepdims=True)
            dy = (dm * s_ref[...]).astype(BF16)
            dw_ref[0] = lax.dot_general(dl, dy, (((0,), (0,)), ((), ())), preferred_element_type=F32)
            ddl = lax.dot_general(dy, wv, (((1,), (1,)), ((), ())), preferred_element_type=F32)
            dxn_ref[...] = _pool_delta_t(ddl, win)
        return branch

    def body(*refs):
        for gi, win in enumerate(POOL_WINDOWS):
            pl.when(pl.program_id(0) == gi)(make_branch(win, *refs))

    blk = pl.BlockSpec((t, POOL_GROUP), lambda g: (0, g))
    wspec = pl.BlockSpec((1, POOL_GROUP, POOL_GROUP), lambda g: (g, 0, 0))
    vec = pl.BlockSpec((1, POOL_GROUP), lambda g: (0, g))
    return pl.pallas_call(
        body, name=name, grid=(len(POOL_WINDOWS),), in_specs=[blk, wspec, vec, blk], out_specs=[blk, wspec, vec],
        out_shape=[jax.ShapeDtypeStruct((t, D_MODEL), F32),
                   jax.ShapeDtypeStruct((len(POOL_WINDOWS), POOL_GROUP, POOL_GROUP), F32),
                   jax.ShapeDtypeStruct((1, D_MODEL), F32)],
        compiler_params=_params(("parallel",)))(xn, w, scale, dmix)


def _qkv_conv(h, wr):
    return (wr[3:4, :] * h + wr[2:3, :] * _shift_down(h, 1) + wr[1:2, :] * _shift_down(h, 2)
            + wr[0:1, :] * _shift_down(h, 3))


def _qkv_pre_fwd(h, w, col0, ncols, normalize, scale, name):
    t = h.shape[0]

    def body(h_ref, w_ref, o_ref):
        c = _qkv_conv(h_ref[...], w_ref[...])
        s = c * _sigmoid(c)
        if normalize:
            s = s * lax.rsqrt(jnp.sum(s * s, axis=-1, keepdims=True) + 1e-6) * scale
        o_ref[...] = s

    return pl.pallas_call(
        body, name=name, grid=(ncols,),
        in_specs=[pl.BlockSpec((t, HEAD_A), lambda j: (0, j + col0)), pl.BlockSpec((CONV_A, HEAD_A), lambda j: (0, j + col0))],
        out_specs=pl.BlockSpec((t, HEAD_A), lambda j: (0, j)), out_shape=jax.ShapeDtypeStruct((t, ncols * HEAD_A), F32),
        compiler_params=_params(("parallel",)))(h, w)


def _qkv_pre_bwd(h, w, dy, col0, ncols, normalize, scale, name):
    t = h.shape[0]

    def body(h_ref, w_ref, dy_ref, dh_ref, dw_ref):
        hv, wr, dyv = h_ref[...], w_ref[...], dy_ref[...]
        h1, h2, h3 = _shift_down(hv, 1), _shift_down(hv, 2), _shift_down(hv, 3)
        c = wr[3:4, :] * hv + wr[2:3, :] * h1 + wr[1:2, :] * h2 + wr[0:1, :] * h3
        s, dsilu = _silu_and_grad(c)
        if normalize:
            r = lax.rsqrt(jnp.sum(s * s, axis=-1, keepdims=True) + 1e-6)
            y = s * r
            dyv = dyv * scale
            ds = r * (dyv - y * jnp.sum(dyv * y, axis=-1, keepdims=True))
        else:
            ds = dyv
        dc = ds * dsilu
        dh = (wr[3:4, :] * dc + wr[2:3, :] * _shift_up(dc, 1) + wr[1:2, :] * _shift_up(dc, 2)
              + wr[0:1, :] * _shift_up(dc, 3))
        dh_ref[...] = dh.astype(BF16)
        dw_ref[0:1, :] = jnp.sum(dc * h3, axis=0, keepdims=True)
        dw_ref[1:2, :] = jnp.sum(dc * h2, axis=0, keepdims=True)
        dw_ref[2:3, :] = jnp.sum(dc * h1, axis=0, keepdims=True)
        dw_ref[3:4, :] = jnp.sum(dc * hv, axis=0, keepdims=True)

    return pl.pallas_call(
        body, name=name, grid=(ncols,),
        in_specs=[pl.BlockSpec((t, HEAD_A), lambda j: (0, j + col0)), pl.BlockSpec((CONV_A, HEAD_A), lambda j: (0, j + col0)),
                  pl.BlockSpec((t, HEAD_A), lambda j: (0, j))],
        out_specs=[pl.BlockSpec((t, HEAD_A), lambda j: (0, j)), pl.BlockSpec((CONV_A, HEAD_A), lambda j: (0, j))],
        out_shape=[jax.ShapeDtypeStruct((t, ncols * HEAD_A), BF16), jax.ShapeDtypeStruct((CONV_A, ncols * HEAD_A), F32)],
        compiler_params=_params(("parallel",)))(h, w, dy)


def _softplus(x):
    return jnp.maximum(x, 0.0) + jnp.log1p(jnp.exp(-jnp.abs(x)))


def _gates_fwd(ba, arow, brow, name):
    t = ba.shape[0]

    def body(x_ref, a_ref, b_ref, o_ref):
        xv = x_ref[...]
        lane = lax.broadcasted_iota(jnp.int32, xv.shape, 1)
        beta = _sigmoid(xv)
        g = -jnp.exp(a_ref[...]) * _softplus(xv + b_ref[...])
        o_ref[...] = jnp.where(lane < N_HEADS_A, beta, jnp.where(lane < 2 * N_HEADS_A, g, 0.0))

    return pl.pallas_call(body, name=name, out_shape=jax.ShapeDtypeStruct((t, LANE), F32),
                          compiler_params=_params())(ba, arow, brow)


def _gates_bwd(ba, arow, brow, dgb, name):
    t = ba.shape[0]

    def body(x_ref, a_ref, b_ref, d_ref, dx_ref, da_ref, db_ref):
        xv = x_ref[...]
        dv = d_ref[0] + d_ref[1] + d_ref[2] + d_ref[3]
        lane = lax.broadcasted_iota(jnp.int32, xv.shape, 1)
        beta = _sigmoid(xv)
        ea = jnp.exp(a_ref[...])
        z = xv + b_ref[...]
        dgv = jnp.where((lane >= N_HEADS_A) & (lane < 2 * N_HEADS_A), dv, 0.0) * (-ea)
        dz = dgv * _sigmoid(z)
        dx = jnp.where(lane < N_HEADS_A, dv * beta * (1.0 - beta), dz)
        dx_ref[...] = dx.astype(BF16)
        db_ref[...] = jnp.sum(dz, axis=0, keepdims=True)
        da_ref[...] = jnp.sum(dgv * _softplus(z), axis=0, keepdims=True)

    return pl.pallas_call(
        body, name=name,
        out_shape=[jax.ShapeDtypeStruct((t, LANE), BF16), jax.ShapeDtypeStruct((1, LANE), F32),
                   jax.ShapeDtypeStruct((1, LANE), F32)],
        compiler_params=_params())(ba, arow, brow, dgb)


def _head_gates(gates, head):
    lane = lax.broadcasted_iota(jnp.int32, gates.shape, 1)
    beta = jnp.sum(jnp.where(lane == head, gates, 0.0), axis=1, keepdims=True)
    g = jnp.sum(jnp.where(lane == head + N_HEADS_A, gates, 0.0), axis=1, keepdims=True)
    return beta, g


_B_NN, _B_NT, _B_TN = ((2,), (1,)), ((2,), (2,)), ((1,), (1,))


def _bdot(a, b, dims=_B_NN, prec=None):
    if prec is None:
        a, b = a.astype(BF16), b.astype(BF16)
    return lax.dot_general(a, b, (dims, ((0,), (0,))), precision=prec, preferred_element_type=F32)


def _heads_of(ref):
    return jnp.stack([ref[:, h * HEAD_A:(h + 1) * HEAD_A] for h in range(N_HEADS_A)])


def _all_head_gates(gates):
    pairs = [_head_gates(gates, h) for h in range(N_HEADS_A)]
    return jnp.stack([b for b, _ in pairs]), jnp.stack([g for _, g in pairs])


def _gdr_terms(k, beta, g):
    h, c = k.shape[0], GDR_CHUNK
    row = lax.broadcasted_iota(jnp.int32, (c, c), 0)
    col = lax.broadcasted_iota(jnp.int32, (c, c), 1)
    causal, strict = row >= col, row > col
    lower = jnp.broadcast_to(causal.astype(F32), (h, c, c))
    gcum = _bdot(lower, jnp.broadcast_to(g, (h, c, c)), prec=HIGHEST)
    diff = gcum - jnp.swapaxes(gcum, 1, 2)
    decay = jnp.where(causal, jnp.exp(jnp.where(causal, diff, 0.0)), 0.0)
    kb = k * beta
    return row, col, causal, strict, gcum, decay, kb, _bdot(kb, k, _B_NT)


def _unit_lower_inverses(a):
    c = a.shape[1]
    eye = (lax.broadcasted_iota(jnp.int32, (c, c), 0) == lax.broadcasted_iota(jnp.int32, (c, c), 1)).astype(F32)
    p = -a
    inv = eye + p
    step = 1
    while 2 * step < c:
        p = _bdot(p, p, prec=HIGH)
        inv = inv + _bdot(inv, p, prec=HIGH)
        step *= 2
    return inv


def _gdr_fwd(q, k, v, gates, name, comm=None):
    t = q.shape[0]
    c, nh = GDR_CHUNK, N_HEADS_A
    n = t // c

    def body(q_ref, k_ref, v_ref, gb_ref, o_ref, tm_ref, s_ref, state):
        @pl.when(pl.program_id(0) == 0)
        def _():
            state[...] = jnp.zeros_like(state)

        qv, kv, vv = _heads_of(q_ref), _heads_of(k_ref), _heads_of(v_ref)
        beta, g = _all_head_gates(gb_ref[...])
        row, col, causal, strict, gcum, decay, kb, kk = _gdr_terms(kv, beta, g)
        tm = _unit_lower_inverses(jnp.where(strict, kk * decay, 0.0))
        e = jnp.exp(gcum)
        u = _bdot(tm, vv * beta, prec=HIGH)
        w = _bdot(tm, kb * e, prec=HIGH)
        p = jnp.where(causal, _bdot(qv, kv, _B_NT) * decay, 0.0)
        s = state[...]
        s_ref[:, 0] = s
        tm_ref[:, 0] = tm
        vn = u - _bdot(w, s)
        o = _bdot(qv * e, s) + _bdot(p, vn)
        for h in range(nh):
            o_ref[:, h * HEAD_A:(h + 1) * HEAD_A] = o[h]
        glast = gcum[:, c - 1:c, :]
        state[...] = s * jnp.exp(glast) + _bdot(kv * jnp.exp(glast - gcum), vn, _B_TN)

    blk = pl.BlockSpec((c, WIDTH_A), lambda i: (i, 0))
    mat = pl.BlockSpec((nh, 1, c, c), lambda i: (0, i, 0, 0))
    return _call(
        body, name=name, grid=(n,), in_specs=[blk, blk, blk, pl.BlockSpec((c, LANE), lambda i: (i, 0))],
        out_specs=[blk, mat, mat],
        out_shape=[jax.ShapeDtypeStruct((t, WIDTH_A), F32), jax.ShapeDtypeStruct((nh, n, c, c), F32),
                   jax.ShapeDtypeStruct((nh, n, HEAD_A, HEAD_A), F32)],
        scratch_shapes=[pltpu.VMEM((nh, HEAD_A, HEAD_A), F32)], sem=("arbitrary",),
        args=(q, k, v, gates), comm=comm)


def _gdr_bwd(q, k, v, gates, tm_all, s_all, do, name, comm=None):
    t = q.shape[0]
    c, nh = GDR_CHUNK, N_HEADS_A
    n = t // c

    def body(q_ref, k_ref, v_ref, gb_ref, tm_ref, s_ref, do_ref, dq_ref, dk_ref, dv_ref, dgb_ref, dstate):
        @pl.when(pl.program_id(0) == 0)
        def _():
            dstate[...] = jnp.zeros_like(dstate)

        qv, kv, vv, dov = _heads_of(q_ref), _heads_of(k_ref), _heads_of(v_ref), _heads_of(do_ref)
        beta, g = _all_head_gates(gb_ref[...])
        tm, s, dsp = tm_ref[:, 0], s_ref[:, 0], dstate[...]
        row, col, causal, strict, gcum, decay, kb, kk = _gdr_terms(kv, beta, g)
        rowsum = lambda x: jnp.sum(x, axis=2, keepdims=True)
        e = jnp.exp(gcum)
        vb, kbe = vv * beta, kb * e
        u = _bdot(tm, vb, prec=HIGH)
        w = _bdot(tm, kbe, prec=HIGH)
        qk = _bdot(qv, kv, _B_NT)
        p = jnp.where(causal, qk * decay, 0.0)
        vn = u - _bdot(w, s)
        glast = gcum[:, c - 1:c, :]
        el = jnp.exp(glast)
        f = jnp.exp(glast - gcum)
        kd = kv * f
        qe = qv * e

        dvn = _bdot(p, dov, _B_TN) + _bdot(kd, dsp)
        dglast = el[:, :, 0:1] * jnp.sum(s * dsp, axis=(1, 2), keepdims=True)
        dkd = _bdot(vn, dsp, _B_NT)
        dk = dkd * f
        df = rowsum(dkd * kv) * f[:, :, 0:1]
        dglast = dglast + jnp.sum(df, axis=1, keepdims=True)
        dgc = -df
        dp = jnp.where(causal, _bdot(dov, vn, _B_NT), 0.0)
        dqe = _bdot(dov, s, _B_NT)
        dq = dqe * e
        de = rowsum(dqe * qv)
        dstate[...] = dsp * el + _bdot(qe, dov, _B_TN) - _bdot(w, dvn, _B_TN)
        dw = -_bdot(dvn, s, _B_NT)
        dvb = _bdot(tm, dvn, _B_TN, prec=HIGH)
        dkbe = _bdot(tm, dw, _B_TN, prec=HIGH)
        da = -jnp.where(strict, _bdot(dvb, u, _B_NT) + _bdot(dkbe, w, _B_NT), 0.0)
        dkk = da * decay
        dqk = dp * decay
        dd = da * kk + dp * qk
        dq = dq + _bdot(dqk, kv)
        dk = dk + _bdot(dqk, qv, _B_TN)
        dkb = _bdot(dkk, kv) + dkbe * e
        dk = dk + _bdot(dkk, kb, _B_TN)
        de = de + rowsum(dkbe * kb)
        dk = dk + dkb * beta
        dbeta = rowsum(dkb * kv) + rowsum(dvb * vv)
        m = dd * decay
        dgc = dgc + rowsum(m) - rowsum(jnp.swapaxes(m, 1, 2))
        dgc = dgc + de * e[:, :, 0:1]
        dgc = dgc + jnp.where(row[:, 0:1] == c - 1, dglast, 0.0)
        upper = jnp.broadcast_to((row <= col).astype(F32), (nh, c, c))
        dg = _bdot(upper, jnp.broadcast_to(dgc, (nh, c, c)), prec=HIGHEST)
        dv = dvb * beta
        for h in range(nh):
            cols = slice(h * HEAD_A, (h + 1) * HEAD_A)
            dq_ref[:, cols] = dq[h]
            dk_ref[:, cols] = dk[h]
            dv_ref[:, cols] = dv[h]
        head = lax.broadcasted_iota(jnp.int32, (nh, c, LANE), 0)
        lane = lax.broadcasted_iota(jnp.int32, (nh, c, LANE), 2)
        dgb_ref[...] = jnp.where(lane == head, dbeta, jnp.where(lane == head + nh, dg, 0.0))

    blk = pl.BlockSpec((c, WIDTH_A), lambda i: (n - 1 - i, 0))
    mat = pl.BlockSpec((nh, 1, c, c), lambda i: (0, n - 1 - i, 0, 0))
    return _call(
        body, name=name, grid=(n,),
        in_specs=[blk, blk, blk, pl.BlockSpec((c, LANE), lambda i: (n - 1 - i, 0)), mat, mat, blk],
        out_specs=[blk, blk, blk, pl.BlockSpec((nh, c, LANE), lambda i: (0, n - 1 - i, 0))],
        out_shape=[jax.ShapeDtypeStruct((t, WIDTH_A), F32)] * 3 + [jax.ShapeDtypeStruct((nh, t, LANE), F32)],
        scratch_shapes=[pltpu.VMEM((nh, HEAD_A, HEAD_A), F32)], sem=("arbitrary",),
        args=(q, k, v, gates, tm_all, s_all, do), comm=comm)


def _onorm_fwd(o, gate, g, name):
    t = o.shape[0]

    def body(o_ref, gate_ref, g_ref, y_ref):
        ov, gv = o_ref[...], gate_ref[...]
        r = lax.rsqrt(jnp.mean(ov * ov, axis=-1, keepdims=True) + RMS_EPS)
        y_ref[...] = (ov * r * g_ref[...] * gv * _sigmoid(gv)).astype(BF16)

    blk = pl.BlockSpec((t, HEAD_A), lambda j: (0, j))
    return pl.pallas_call(
        body, name=name, grid=(N_HEADS_A,), in_specs=[blk, blk, pl.BlockSpec((1, HEAD_A), lambda j: (0, 0))],
        out_specs=blk, out_shape=jax.ShapeDtypeStruct((t, WIDTH_A), BF16),
        compiler_params=_params(("parallel",)))(o, gate, g)


def _onorm_bwd(o, gate, g, dy, name):
    t = o.shape[0]

    def body(o_ref, gate_ref, g_ref, dy_ref, do_ref, dgate_ref, dg_ref):
        @pl.when(pl.program_id(0) == 0)
        def _():
            dg_ref[...] = jnp.zeros_like(dg_ref)

        ov, gv, dyv = o_ref[...], gate_ref[...], dy_ref[...].astype(F32)
        r = lax.rsqrt(jnp.mean(ov * ov, axis=-1, keepdims=True) + RMS_EPS)
        oh = ov * r
        sg, dsg = _silu_and_grad(gv)
        dgate_ref[...] = (dyv * oh * g_ref[...] * dsg).astype(BF16)
        dn = dyv * sg
        dg_ref[...] += jnp.sum(dn * oh, axis=0, keepdims=True)
        dng = dn * g_ref[...]
        do_ref[...] = r * (dng - oh * jnp.mean(dng * oh, axis=-1, keepdims=True))

    blk = pl.BlockSpec((t, HEAD_A), lambda j: (0, j))
    vec = pl.BlockSpec((1, HEAD_A), lambda j: (0, 0))
    return pl.pallas_call(
        body, name=name, grid=(N_HEADS_A,), in_specs=[blk, blk, vec, blk], out_specs=[blk, blk, vec],
        out_shape=[jax.ShapeDtypeStruct((t, WIDTH_A), F32), jax.ShapeDtypeStruct((t, WIDTH_A), BF16),
                   jax.ShapeDtypeStruct((1, HEAD_A), F32)],
        compiler_params=_params(("arbitrary",)))(o, gate, g, dy)


def _cmul(ar, ai, br, bi):
    return ar * br - ai * bi, ar * bi + ai * br


def _scan_tables(ar, ai, reverse):
    p1 = (ar, ai)
    p2 = _cmul(*p1, *p1)
    p4 = _cmul(*p2, *p2)
    p8 = _cmul(*p4, *p4)
    p3 = _cmul(*p2, *p1)
    p5 = _cmul(*p4, *p1)
    p6 = _cmul(*p4, *p2)
    p7 = _cmul(*p4, *p3)
    pows = [p1, p2, p3, p4, p5, p6, p7, p8]
    rows = lax.broadcasted_iota(jnp.int32, (8, ar.shape[1]), 0)
    tr = jnp.zeros((8, ar.shape[1]), F32)
    ti = jnp.zeros((8, ar.shape[1]), F32)
    for r in range(8):
        pw = pows[7 - r] if reverse else pows[r]
        tr = jnp.where(rows == r, pw[0], tr)
        ti = jnp.where(rows == r, pw[1], ti)
    return p1, p2, p4, p8, tr, ti


def _tile_scan(xr, xi, p1, p2, p4, reverse):
    rows = lax.broadcasted_iota(jnp.int32, xr.shape, 0)
    for s, (pr, pi) in ((1, p1), (2, p2), (4, p4)):
        if reverse:
            keep = rows < 8 - s
            sr, si = pltpu.roll(xr, 8 - s, 0), pltpu.roll(xi, 8 - s, 0)
        else:
            keep = rows >= s
            sr, si = pltpu.roll(xr, s, 0), pltpu.roll(xi, s, 0)
        sr, si = jnp.where(keep, sr, 0.0), jnp.where(keep, si, 0.0)
        mr, mi = _cmul(pr, pi, sr, si)
        xr, xi = xr + mr, xi + mi
    return xr, xi


def _s5_scan_fwd(bu, a, name, tb=512, comm=None):
    t = bu.shape[0]
    cb = SCAN_CB
    nt = t // tb

    def body(b_ref, a_ref, x_ref, carry):
        @pl.when(pl.program_id(1) == 0)
        def _():
            carry[...] = jnp.zeros_like(carry)

        ar, ai = a_ref[:, 0:cb], a_ref[:, cb:2 * cb]
        p1, p2, p4, p8, tr, ti = _scan_tables(ar, ai, False)

        def step(j, c):
            cr, ci = c
            i = pl.multiple_of(j * 8, 8)
            xr, xi = _tile_scan(b_ref[pl.ds(i, 8), 0:cb], b_ref[pl.ds(i, 8), cb:2 * cb], p1, p2, p4, False)
            mr, mi = _cmul(tr, ti, cr, ci)
            xr, xi = xr + mr, xi + mi
            x_ref[pl.ds(i, 8), 0:cb] = xr
            x_ref[pl.ds(i, 8), cb:2 * cb] = xi
            return xr[7:8, :], xi[7:8, :]

        cr, ci = lax.fori_loop(0, tb // 8, step, (carry[0:1, :], carry[1:2, :]), unroll=2)
        carry[0:1, :] = cr
        carry[1:2, :] = ci

    blk = pl.BlockSpec((tb, 2 * cb), lambda j, i: (i, j))
    return _call(
        body, name=name, grid=(SSM_CH // cb, nt),
        in_specs=[blk, pl.BlockSpec((1, 2 * cb), lambda j, i: (0, j))], out_specs=blk,
        out_shape=jax.ShapeDtypeStruct((t, 2 * SSM_CH), F32), scratch_shapes=[pltpu.VMEM((8, cb), F32)],
        sem=("parallel", "arbitrary"), args=(bu, a), comm=comm)


def _s5_scan_bwd(dx, x, a, name, tb=512, comm=None):
    t = dx.shape[0]
    cb = SCAN_CB
    nt = t // tb
    nj = tb // 8

    def body(d_ref, x_ref, xp_ref, a_ref, l_ref, da_ref, carry, acc):
        tblk = pl.program_id(1)

        @pl.when(tblk == 0)
        def _():
            carry[...] = jnp.zeros_like(carry)
            acc[...] = jnp.zeros_like(acc)

        ar, ai = a_ref[:, 0:cb], a_ref[:, cb:2 * cb]
        p1, p2, p4, p8, tr, ti = _scan_tables(ar, -ai, True)
        rows = lax.broadcasted_iota(jnp.int32, (8, cb), 0)

        def step(jj, c):
            cr, ci, sr_acc, si_acc = c
            j = nj - 1 - jj
            i = pl.multiple_of(j * 8, 8)
            lr, li = _tile_scan(d_ref[pl.ds(i, 8), 0:cb], d_ref[pl.ds(i, 8), cb:2 * cb], p1, p2, p4, True)
            mr, mi = _cmul(tr, ti, cr, ci)
            lr, li = lr + mr, li + mi
            l_ref[pl.ds(i, 8), 0:cb] = lr
            l_ref[pl.ds(i, 8), cb:2 * cb] = li
            ip = pl.multiple_of(jnp.maximum(j - 1, 0) * 8, 8)
            prev_r = jnp.where(j > 0, x_ref[pl.ds(ip, 8), 0:cb], xp_ref[:, 0:cb])
            prev_i = jnp.where(j > 0, x_ref[pl.ds(ip, 8), cb:2 * cb], xp_ref[:, cb:2 * cb])
            edge = jnp.where(jnp.logical_and(j == 0, tblk == nt - 1), 0.0, 1.0)
            xs_r = jnp.where(rows == 0, pltpu.roll(prev_r, 1, 0) * edge, pltpu.roll(x_ref[pl.ds(i, 8), 0:cb], 1, 0))
            xs_i = jnp.where(rows == 0, pltpu.roll(prev_i, 1, 0) * edge, pltpu.roll(x_ref[pl.ds(i, 8), cb:2 * cb], 1, 0))
            sr_acc = sr_acc + lr * xs_r + li * xs_i
            si_acc = si_acc + li * xs_r - lr * xs_i
            return lr[0:1, :], li[0:1, :], sr_acc, si_acc

        cr, ci, sr_acc, si_acc = lax.fori_loop(
            0, nj, step, (carry[0:1, :], carry[1:2, :], acc[:, 0:cb], acc[:, cb:2 * cb]))
        carry[0:1, :] = cr
        carry[1:2, :] = ci
        acc[:, 0:cb] = sr_acc
        acc[:, cb:2 * cb] = si_acc

        @pl.when(tblk == nt - 1)
        def _():
            da_ref[...] = jnp.sum(acc[...], axis=0, keepdims=True)

    blk = pl.BlockSpec((tb, 2 * cb), lambda j, i: (nt - 1 - i, j))
    prev = pl.BlockSpec((8, 2 * cb), lambda j, i: (jnp.maximum((nt - 1 - i) * (tb // 8) - 1, 0), j))
    vec = pl.BlockSpec((1, 2 * cb), lambda j, i: (0, j))
    return _call(
        body, name=name, grid=(SSM_CH // cb, nt), in_specs=[blk, blk, prev, vec], out_specs=[blk, vec],
        out_shape=[jax.ShapeDtypeStruct((t, 2 * SSM_CH), F32), jax.ShapeDtypeStruct((1, 2 * SSM_CH), F32)],
        scratch_shapes=[pltpu.VMEM((8, cb), F32), pltpu.VMEM((8, 2 * cb), F32)],
        sem=("parallel", "arbitrary"), args=(dx, x, x, a), comm=comm)


def _glu_fwd(yc, u, dvec, wg, bg, name, tr=256):
    t = yc.shape[0]

    def body(yc_ref, u_ref, d_ref, w_ref, b_ref, yl_ref, yb_ref):
        yl = yc_ref[...] + d_ref[...] * u_ref[...]
        yl_ref[...] = yl
        yg, _ = _gelu_and_grad(yl)
        z = jnp.dot(yg.astype(BF16), w_ref[...], preferred_element_type=F32) + b_ref[...]
        yb_ref[...] = (yg * _sigmoid(z)).astype(BF16)

    blk = pl.BlockSpec((tr, SSM_WIDTH), lambda i: (i, 0))
    vec = pl.BlockSpec((1, SSM_WIDTH), lambda i: (0, 0))
    return pl.pallas_call(
        body, name=name, grid=(t // tr,),
        in_specs=[blk, blk, vec, pl.BlockSpec((SSM_WIDTH, SSM_WIDTH), lambda i: (0, 0)), vec],
        out_specs=[blk, blk],
        out_shape=[jax.ShapeDtypeStruct((t, SSM_WIDTH), F32), jax.ShapeDtypeStruct((t, SSM_WIDTH), BF16)],
        compiler_params=_params(("parallel",)))(yc, u, dvec, wg, bg)


def _glu_bwd(yl, u, dvec, wg, bg, dyb, name, tr=256):
    t = yl.shape[0]

    def body(yl_ref, u_ref, d_ref, w_ref, b_ref, dy_ref, dyl_ref, du_ref, dw_ref, db_ref, dd_ref):
        @pl.when(pl.program_id(0) == 0)
        def _():
            dw_ref[...] = jnp.zeros_like(dw_ref)
            db_ref[...] = jnp.zeros_like(db_ref)
            dd_ref[...] = jnp.zeros_like(dd_ref)

        ylv, dyv, wv = yl_ref[...], dy_ref[...].astype(F32), w_ref[...]
        yg, dgelu = _gelu_and_grad(ylv)
        ygb = yg.astype(BF16)
        z = jnp.dot(ygb, wv, preferred_element_type=F32) + b_ref[...]
        sg = _sigmoid(z)
        dz = dyv * yg * sg * (1.0 - sg)
        dzb = dz.astype(BF16)
        dyg = dyv * sg + lax.dot_general(dzb, wv, (((1,), (1,)), ((), ())), preferred_element_type=F32)
        dyl = dyg * dgelu
        dyl_ref[...] = dyl.astype(BF16)
        du_ref[...] = dyl * d_ref[...]
        dw_ref[...] += lax.dot_general(ygb, dzb, (((0,), (0,)), ((), ())), preferred_element_type=F32)
        db_ref[...] += jnp.sum(dz, axis=0, keepdims=True)
        dd_ref[...] += jnp.sum(dyl * u_ref[...], axis=0, keepdims=True)

    blk = pl.BlockSpec((tr, SSM_WIDTH), lambda i: (i, 0))
    vec = pl.BlockSpec((1, SSM_WIDTH), lambda i: (0, 0))
    wsp = pl.BlockSpec((SSM_WIDTH, SSM_WIDTH), lambda i: (0, 0))
    return pl.pallas_call(
        body, name=name, grid=(t // tr,), in_specs=[blk, blk, vec, wsp, vec, blk],
        out_specs=[blk, blk, wsp, vec, vec],
        out_shape=[jax.ShapeDtypeStruct((t, SSM_WIDTH), BF16), jax.ShapeDtypeStruct((t, SSM_WIDTH), F32),
                   jax.ShapeDtypeStruct((SSM_WIDTH, SSM_WIDTH), F32), jax.ShapeDtypeStruct((1, SSM_WIDTH), F32),
                   jax.ShapeDtypeStruct((1, SSM_WIDTH), F32)],
        compiler_params=_params(("arbitrary",)))(yl, u, dvec, wg, bg, dyb)


def _mesh_pos():
    return lax.axis_index("x"), lax.axis_index("y"), lax.axis_index("c")


def _device_index():
    x, y, c = _mesh_pos()
    return 4 * x + 2 * y + c


def _gather_comm(arrays):
    na = len(arrays)

    def own_copy(ins, outs, sems, ai):
        return pltpu.make_async_copy(ins[ai], outs[ai].at[_device_index()], sems[2].at[ai])

    def ctx(ins, outs, sems):
        send_sems, recv_sems = sems[:2]
        x, y, c = _mesh_pos()
        chips = [(1 - x, y), (x, 1 - y), (1 - x, 1 - y)]

        def copy(ai, kk, block, to, own=False):
            slot = outs[ai].at[4 * block[0] + 2 * block[1] + block[2]]
            return pltpu.make_async_remote_copy(
                src_ref=ins[ai] if own else slot, dst_ref=slot, send_sem=send_sems.at[ai, kk],
                recv_sem=recv_sems.at[ai, kk], device_id=to, device_id_type=MESH)

        return (x, y, c), (x, y, 1 - c), chips, c, copy

    def start(ins, outs, sems):
        me, sibling, chips, c, copy = ctx(ins, outs, sems)
        for ai in range(na):
            copy(ai, 0, me, sibling, own=True).start()
            for j, chip in enumerate(chips):
                copy(ai, 1 + j, me, (*chip, c), own=True).start()
        for ai in range(na):
            own_copy(ins, outs, sems, ai).start()

    def mid(ins, outs, sems):
        me, sibling, chips, c, copy = ctx(ins, outs, sems)
        for ai in range(na):
            for j, chip in enumerate(chips):
                copy(ai, 1 + j, (*chip, c), me).wait_recv()
                copy(ai, 4 + j, (*chip, c), sibling).start()

    def end(ins, outs, sems):
        me, sibling, chips, c, copy = ctx(ins, outs, sems)
        for ai in range(na):
            copy(ai, 0, sibling, me).wait_recv()
            copy(ai, 0, me, sibling, own=True).wait_send()
            for j, chip in enumerate(chips):
                copy(ai, 4 + j, (*chip, 1 - c), me).wait_recv()
                copy(ai, 1 + j, me, (*chip, c), own=True).wait_send()
                copy(ai, 4 + j, (*chip, c), sibling).wait_send()
            own_copy(ins, outs, sems, ai).wait()

    return Comm(arrays, [jax.ShapeDtypeStruct((N_DEV,) + a.shape, a.dtype) for a in arrays],
                [pltpu.SemaphoreType.DMA((na, 7)), pltpu.SemaphoreType.DMA((na, 7)), pltpu.SemaphoreType.DMA((na,))],
                start, end, mid)


def _sequencer_gather(arrays, name, collective_id):
    comm = _gather_comm(arrays)
    na = len(arrays)

    def body(*refs):
        ins, outs, sems = refs[:na], refs[na:2 * na], refs[2 * na:]
        x, y, c = _mesh_pos()
        peers = [(x, y, 1 - c), (1 - x, y, c), (x, 1 - y, c), (1 - x, 1 - y, c)]
        barrier = pltpu.get_barrier_semaphore()
        for peer in peers:
            pl.semaphore_signal(barrier, inc=1, device_id=peer, device_id_type=MESH)
        pl.semaphore_wait(barrier, len(peers))
        comm.start(ins, outs, sems)
        comm.mid(ins, outs, sems)
        comm.end(ins, outs, sems)

    return list(pl.kernel(
        body, out_type=tuple(comm.out_shapes), mesh=plsc.ScalarSubcoreMesh(axis_name="sequencer", num_cores=1),
        name=name, scratch_types=tuple(comm.sems),
        compiler_params=pltpu.CompilerParams(collective_id=collective_id))(*arrays))


def _sequencer_exchange(comm, peers_of, name, collective_id):
    na = len(comm.inputs)

    def body(*refs):
        ins, outs, sems = refs[:na], refs[na:na + len(comm.out_shapes)], refs[na + len(comm.out_shapes):]
        peers = peers_of(*_mesh_pos())
        barrier = pltpu.get_barrier_semaphore()
        for peer in peers:
            pl.semaphore_signal(barrier, inc=1, device_id=peer, device_id_type=MESH)
        pl.semaphore_wait(barrier, len(peers))
        comm.start(ins, outs, sems)
        comm.end(ins, outs, sems)

    return list(pl.kernel(
        body, out_type=tuple(comm.out_shapes), mesh=plsc.ScalarSubcoreMesh(axis_name="sequencer", num_cores=1),
        name=name, scratch_types=tuple(comm.sems),
        compiler_params=pltpu.CompilerParams(collective_id=collective_id))(*comm.inputs))


SIBLING_SWAP_ID, CHIP_EXCHANGE_ID = 9, 10


def _sequencer_swap(arrays, name):
    return _sequencer_exchange(_swap_comm(arrays), lambda x, y, c: [(x, y, 1 - c)], name, SIBLING_SWAP_ID)[0]


def _sequencer_chips(send, name):
    return _sequencer_exchange(_chips_comm(send), lambda x, y, c: [(1 - x, y, c), (x, 1 - y, c), (1 - x, 1 - y, c)],
                               name, CHIP_EXCHANGE_ID)[0]


def _swap_comm(arrays):
    na = len(arrays)
    offs = np.concatenate([[0], np.cumsum([a.shape[1] for a in arrays])]).astype(int)

    def copies(ins, outs, sems):
        x, y, c = _mesh_pos()
        return [pltpu.make_async_remote_copy(
            src_ref=ins[ai].at[2 * k + 1 - c], dst_ref=outs[0].at[k, pl.ds(int(offs[ai]), arrays[ai].shape[1])],
            send_sem=sems[0].at[ai, k], recv_sem=sems[1].at[ai, k], device_id=(x, y, 1 - c), device_id_type=MESH)
            for ai in range(na) for k in range(4)]

    def start(ins, outs, sems):
        for cp in copies(ins, outs, sems):
            cp.start()

    def end(ins, outs, sems):
        for cp in copies(ins, outs, sems):
            cp.wait()

    return Comm(arrays, [jax.ShapeDtypeStruct((4, int(offs[-1]), PACK_COLS), arrays[0].dtype)],
                [pltpu.SemaphoreType.DMA((na, 4)), pltpu.SemaphoreType.DMA((na, 4))], start, end)


def _chips_comm(send):
    def copies(ins, outs, sems):
        x, y, c = _mesh_pos()
        chips = [(1 - x, y), (x, 1 - y), (1 - x, 1 - y)]
        return [pltpu.make_async_remote_copy(
            src_ref=ins[0].at[2 * cx + cy], dst_ref=outs[0].at[j], send_sem=sems[0].at[j], recv_sem=sems[1].at[j],
            device_id=(cx, cy, c), device_id_type=MESH) for j, (cx, cy) in enumerate(chips)]

    def start(ins, outs, sems):
        for cp in copies(ins, outs, sems):
            cp.start()

    def end(ins, outs, sems):
        for cp in copies(ins, outs, sems):
            cp.wait()

    return Comm([send], [jax.ShapeDtypeStruct((3,) + send.shape[1:], send.dtype)],
                [pltpu.SemaphoreType.DMA((3,)), pltpu.SemaphoreType.DMA((3,))], start, end)


def _pair_sum(keep, recv, name, tr=464):
    nchip, rows, cols = keep.shape

    def body(g_ref, r_ref, o_ref):
        o_ref[...] = (g_ref[...].astype(F32) + r_ref[...].astype(F32)).astype(BF16)

    blk = pl.BlockSpec((1, tr, cols), lambda k, i: (k, i, 0))
    return pl.pallas_call(
        body, name=name, grid=(nchip, rows // tr), in_specs=[blk, blk], out_specs=blk,
        out_shape=jax.ShapeDtypeStruct((nchip, rows, cols), BF16),
        compiler_params=_params(("parallel", "parallel")))(keep, recv)


def _pair_sum_pieces(pieces, recv, name, tr):
    _, rows, cols = pieces.shape
    core = lax.axis_index("c").astype(jnp.int32).reshape(1)

    def body(c_ref, g_ref, r_ref, o_ref):
        del c_ref
        o_ref[...] = (g_ref[...].astype(F32) + r_ref[...].astype(F32)).astype(BF16)

    grid_spec = pltpu.PrefetchScalarGridSpec(
        num_scalar_prefetch=1, grid=(4, rows // tr),
        in_specs=[pl.BlockSpec((1, tr, cols), lambda k, i, c_ref: (2 * k + c_ref[0], i, 0)),
                  pl.BlockSpec((1, tr, cols), lambda k, i, c_ref: (k, i, 0))],
        out_specs=pl.BlockSpec((1, tr, cols), lambda k, i, c_ref: (k, i, 0)))
    return pl.pallas_call(
        body, name=name, grid_spec=grid_spec, out_shape=jax.ShapeDtypeStruct((4, rows, cols), BF16),
        compiler_params=_params(("parallel", "parallel")))(core, pieces, recv)


def _chip_sum(own, others, name, tr=464):
    _, rows, cols = own.shape
    chip = (2 * lax.axis_index("x") + lax.axis_index("y")).astype(jnp.int32).reshape(1)

    def body(chip_ref, own_ref, oth_ref, o_ref):
        del chip_ref
        acc = own_ref[0].astype(F32)
        for j in range(3):
            acc = acc + oth_ref[j].astype(F32)
        o_ref[...] = acc

    grid_spec = pltpu.PrefetchScalarGridSpec(
        num_scalar_prefetch=1, grid=(rows // tr,),
        in_specs=[pl.BlockSpec((1, tr, cols), lambda i, chip_ref: (chip_ref[0], i, 0)),
                  pl.BlockSpec((3, tr, cols), lambda i, chip_ref: (0, i, 0))],
        out_specs=pl.BlockSpec((tr, cols), lambda i, chip_ref: (i, 0)))
    return pl.pallas_call(
        body, name=name, grid_spec=grid_spec, out_shape=jax.ShapeDtypeStruct((rows, cols), F32),
        compiler_params=_params(("parallel",)))(chip, own, others)


def _sum_leading(parts, name, tr=464):
    nparts, rows, cols = parts.shape
    tr = tr if rows % tr == 0 else rows

    def body(p_ref, o_ref):
        acc = p_ref[0].astype(F32)
        for i in range(1, nparts):
            acc = acc + p_ref[i].astype(F32)
        o_ref[...] = acc

    return pl.pallas_call(
        body, name=name, grid=(rows // tr,),
        in_specs=[pl.BlockSpec((nparts, tr, cols), lambda i: (0, i, 0))],
        out_specs=pl.BlockSpec((tr, cols), lambda i: (i, 0)), out_shape=jax.ShapeDtypeStruct((rows, cols), F32),
        compiler_params=_params(("parallel",)))(parts)


def _adamw(w, g, m, v, name, comm=None):
    shape = w.shape
    cols = shape[-1]
    lead = shape[0] if len(shape) >= 3 else 1
    rows = int(np.prod(shape[:-1])) // lead if len(shape) > 1 else 1
    w2, g2, m2, v2 = (a.reshape(lead, rows, cols) for a in (w, g, m, v))
    tr = rows
    for cand in (512, 256, 128, 64, 32, 16, 8):
        if rows % cand == 0 and rows > cand:
            tr = cand
            break
    bc1, bc2 = 1.0 - ADAM_B1 ** ADAM_STEP, 1.0 - ADAM_B2 ** ADAM_STEP

    def body(w_ref, g_ref, m_ref, v_ref, d_ref, nm_ref, nv_ref):
        gv = g_ref[...]
        nm = ADAM_B1 * m_ref[...] + (1.0 - ADAM_B1) * gv
        nv = ADAM_B2 * v_ref[...] + (1.0 - ADAM_B2) * (gv * gv)
        nm_ref[...] = nm
        nv_ref[...] = nv
        d_ref[...] = -ADAM_LR * ((nm / bc1) / (jnp.sqrt(nv / bc2) + ADAM_EPS) + ADAM_WD * w_ref[...])

    blk = pl.BlockSpec((1, tr, cols), lambda l, i: (l, i, 0))
    res = _call(body, name=name, grid=(lead, rows // tr), in_specs=[blk] * 4, out_specs=[blk] * 3,
                out_shape=[jax.ShapeDtypeStruct((lead, rows, cols), F32)] * 3, sem=("parallel", "parallel"),
                args=(w2, g2, m2, v2), comm=comm)
    outs, couts = res if comm is not None else (res, None)
    outs = tuple(o.reshape(shape) for o in outs)
    return outs if comm is None else (outs, couts)


WEIGHT_NAMES = ['norm_mix_g', 'norm_xa_g', 'norm_ffn_g', 'norm_mem_g', 'norm_final_g', 'w_in_ab', 'conv_qkv_a',
                'a_log_a', 'dt_bias_a', 'onorm_g_a', 'ssm_lambda_re', 'ssm_lambda_im', 'ssm_b_re', 'ssm_b_im',
                'ssm_c_re', 'ssm_c_im', 'ssm_d', 'ssm_log_dt', 'w_glu_b', 'b_glu_b', 'w_out_ab', 'pool_w',
                'pool_scale', 'xa_wq', 'xa_wkv', 'xa_wo', 'ffn_w_up', 'ffn_conv', 'ffn_w_down']
BIG_SHARDED = {'w_in_ab': ((1, 1024, 2568), 2), 'w_glu_b': ((1, 512, 512), 1), 'w_out_ab': ((1, 1024, 1024), 1),
               'pool_w': ((1, 4, 256, 256), 2), 'xa_wq': ((2, 1024, 1024), 1), 'xa_wkv': ((2, 1024, 2048), 2),
               'xa_wo': ((2, 1024, 1024), 1), 'ffn_w_up': ((2, 1024, 5632), 2), 'ffn_w_down': ((2, 2816, 1024), 1)}
SMALL_SHARDED = {'conv_qkv_a': ((1, 4, 1536), 2), 'pool_scale': ((1, 1024), 1), 'ffn_conv': ((2, 3, 5632), 2)}
REPLICATED = {'norm_mix_g': (2, 1024), 'norm_xa_g': (2, 1024), 'norm_ffn_g': (2, 1024), 'norm_mem_g': (1024,),
              'norm_final_g': (1024,), 'a_log_a': (1, 4), 'dt_bias_a': (1, 4), 'onorm_g_a': (1, 128),
              'ssm_lambda_re': (1, 32, 64), 'ssm_lambda_im': (1, 32, 64), 'ssm_b_re': (1, 32, 64, 16),
              'ssm_b_im': (1, 32, 64, 16), 'ssm_c_re': (1, 32, 16, 64), 'ssm_c_im': (1, 32, 16, 64),
              'ssm_d': (1, 32, 16), 'ssm_log_dt': (1, 32), 'b_glu_b': (1, 512)}
PACK_ROW_ALIGN = 8


def _shard_shape(shape, axis):
    return tuple(s // N_DEV if i == axis else s for i, s in enumerate(shape))


def _round_up(n, m):
    return (n + m - 1) // m * m


def _pack(arrays):
    total = sum(int(np.prod(a.shape)) for a in arrays)
    padded = _round_up(total, PACK_COLS * PACK_ROW_ALIGN)
    parts = [a.astype(F32).reshape(-1) for a in arrays]
    if padded != total:
        parts.append(jnp.zeros((padded - total,), F32))
    return jnp.concatenate(parts).reshape(padded // PACK_COLS, PACK_COLS)


def _unpack(packed, shapes):
    flat, out, off = packed.reshape(-1), [], 0
    for shape in shapes:
        size = int(np.prod(shape))
        out.append(flat[off:off + size].reshape(shape))
        off += size
    return out


def _split_shards(full, axis):
    shape = full.shape
    s = shape[axis] // N_DEV
    a = full.reshape(shape[:axis] + (N_DEV, s) + shape[axis + 1:])
    return jnp.moveaxis(a, axis, 0).reshape(N_DEV, -1)


def _merge_shards(pieces, shape, axis):
    sh = _shard_shape(shape, axis)
    a = pieces.reshape((N_DEV,) + sh)
    a = jnp.moveaxis(a, 0, axis)
    return a.reshape(shape)


_SCAN_NB = SSM_CH // SCAN_CB


def _to_scan_layout(m, axis):
    shape = m.shape
    m = m.reshape(shape[:axis] + (2, _SCAN_NB, SCAN_CB) + shape[axis + 1:])
    return jnp.swapaxes(m, axis, axis + 1).reshape(shape)


def _from_scan_layout(m, axis):
    shape = m.shape
    m = m.reshape(shape[:axis] + (_SCAN_NB, 2, SCAN_CB) + shape[axis + 1:])
    return jnp.swapaxes(m, axis, axis + 1).reshape(shape)


def _s5_discretise(lam_re, lam_im, b_re, b_im, log_dt):
    dt = jnp.exp(log_dt)[:, None]
    mag = jnp.exp(lam_re * dt)
    ang = lam_im * dt
    lb_re, lb_im = mag * jnp.cos(ang), mag * jnp.sin(ang)
    den = lam_re * lam_re + lam_im * lam_im
    nr, ni = lb_re - 1.0, lb_im
    coef_re = (nr * lam_re + ni * lam_im) / den
    coef_im = (ni * lam_re - nr * lam_im) / den
    bb_re = coef_re[..., None] * b_re - coef_im[..., None] * b_im
    bb_im = coef_re[..., None] * b_im + coef_im[..., None] * b_re
    return lb_re, lb_im, bb_re, bb_im


_GROUPS_PER_BLOCK = N_GROUPS // _SCAN_NB
_U_BLOCK = _GROUPS_PER_BLOCK * SSM_GROUP


def _s5_matrices(lb_re, lb_im, bb_re, bb_im, c_re, c_im):
    eye = jnp.eye(_GROUPS_PER_BLOCK, dtype=F32)
    blocked = lambda m: m.reshape((_SCAN_NB, _GROUPS_PER_BLOCK) + m.shape[1:])
    bmat = lambda bb: jnp.einsum('jgph,gk->jghkp', blocked(bb), eye).reshape(_SCAN_NB, _U_BLOCK, SCAN_CB)
    cmat = lambda cc: jnp.einsum('jghp,gk->jkpgh', blocked(cc), eye).reshape(_SCAN_NB, SCAN_CB, _U_BLOCK)
    b_in = jnp.concatenate([bmat(bb_re), bmat(bb_im)], axis=2)
    c_out = jnp.concatenate([cmat(c_re), -cmat(c_im)], axis=1)
    a_row = _to_scan_layout(jnp.concatenate([lb_re.reshape(1, SSM_CH), lb_im.reshape(1, SSM_CH)], axis=1), 1)
    return b_in, c_out, a_row


def _s5_matrix_grads(db_in, dc_out, da_row):
    da_nat = _from_scan_layout(da_row, 1)
    eye = jnp.eye(_GROUPS_PER_BLOCK, dtype=F32)
    nb, gb = _SCAN_NB, _GROUPS_PER_BLOCK
    bgrad = lambda m: jnp.einsum('jghkp,gk->jgph', m.reshape(nb, gb, SSM_GROUP, gb, SSM_STATE), eye
                                 ).reshape(N_GROUPS, SSM_STATE, SSM_GROUP)
    cgrad = lambda m: jnp.einsum('jkpgh,gk->jghp', m.reshape(nb, gb, SSM_STATE, gb, SSM_GROUP), eye
                                 ).reshape(N_GROUPS, SSM_GROUP, SSM_STATE)
    dbb_re, dbb_im = bgrad(db_in[:, :, :SCAN_CB]), bgrad(db_in[:, :, SCAN_CB:])
    dc_re, dc_im = cgrad(dc_out[:, :SCAN_CB]), -cgrad(dc_out[:, SCAN_CB:])
    dlb_re = da_nat[0, :SSM_CH].reshape(N_GROUPS, SSM_STATE)
    dlb_im = da_nat[0, SSM_CH:].reshape(N_GROUPS, SSM_STATE)
    return dlb_re, dlb_im, dbb_re, dbb_im, dc_re, dc_im


def _as_pieces(a):
    return a.reshape(N_DEV, a.shape[0] // N_DEV, a.shape[1])


def _hybrid_fwd(xn, x, wts, p, weights, riders):
    sv = {}
    hq = _mm(xn, wts['w_qkv_t'], "nt", "l0_in_qkv")
    gate = _mm(xn, wts['w_gate_t'], "nt", "l0_in_gate")
    ba = _mm(xn, wts['w_ba_t'], "nt", "l0_in_ba")
    u = _mm(xn, wts['w_u_t'], "nt", "l0_in_u")
    conv = p['conv_qkv']
    q = _qkv_pre_fwd(hq, conv, 0, 4, True, HEAD_A ** -0.5, "l0_q_pre")
    k = _qkv_pre_fwd(hq, conv, 4, 4, True, 1.0, "l0_k_pre")
    v = _qkv_pre_fwd(hq, conv, 8, 4, False, 1.0, "l0_v_pre")
    gates = _gates_fwd(ba, p['arow'], p['brow'], "l0_gates")
    o, tm_all, s_all = riders.run("l0_gdr_fwd", _gdr_fwd, q, k, v, gates)
    wts['w_glu'], wts['w_out'] = weights.full['w_glu'], weights.full['w_out']
    y_a = _onorm_fwd(o, gate, p['onorm_g'], "l0_onorm")
    bu = riders.run("l0_s5_bu", _mm_bd, u, p['b_in'], "nn")
    xs = riders.run("l0_s5_scan", _s5_scan_fwd, bu, p['a_row'])
    weights.gather_by_sequencer(GATHER_LAYER1, xs, "gather_layer1", GATHER_LAYER1_ID)
    yc = riders.run("l0_s5_cx", _mm_bd, xs, p['c_out'], "nn")
    yl, y_b = _glu_fwd(yc, u, p['d_row'], wts['w_glu'], p['b_glu'], "l0_glu")
    mixed = jnp.concatenate([y_a, y_b], axis=1)
    x1 = _mm(mixed, wts['w_out'], "nn", "l0_out", res=x)
    sv.update(hq=hq, gate=gate, ba=ba, u=u, q=q, k=k, v=v, gb=gates, o=o, tm=tm_all, s=s_all, xs=xs, yl=yl, mixed=mixed)
    return x1, sv


def _hybrid_bwd(dx1, xn, wts, p, sv, riders):
    gr = {}
    dmixed = _mm(dx1, wts['w_out'], "nt", "l0_out_dx", out_dtype=BF16)
    riders.grad('w_out', _as_pieces(_mm(sv['mixed'], dx1, "tn", "l0_out_dw", out_dtype=BF16)))
    dya, dyb = dmixed[:, :WIDTH_A], dmixed[:, WIDTH_A:]
    dyl, du_direct, dw_glu, gr['b_glu_b'], dd = _glu_bwd(
        sv['yl'], sv['u'], p['d_row'], wts['w_glu'], p['b_glu'], dyb, "l0_glu_bwd")
    riders.grad('w_glu', dw_glu.astype(BF16).reshape(N_DEV, -1, PACK_COLS))
    dxs = riders.run("l0_s5_cx_dx", _mm_bd, dyl, p['c_out'], "nt")
    dc_out = _mm_bd(sv['xs'], dyl, "tn", "l0_s5_cx_dw")
    lam, da_row = riders.run("l0_s5_scan_bwd", _s5_scan_bwd, dxs, sv['xs'], p['a_row'])
    du = _mm_bd(lam, p['b_in'], "nt", "l0_s5_bu_dx", res=du_direct, out_dtype=BF16)
    db_in = _mm_bd(sv['u'], lam, "tn", "l0_s5_bu_dw")
    gr['s5'] = (db_in, dc_out, da_row, dd)
    do, dgate, gr['onorm_g_a'] = _onorm_bwd(sv['o'], sv['gate'], p['onorm_g'], dya, "l0_onorm_bwd")
    dq, dk, dv, dgb = riders.run("l0_gdr_bwd", _gdr_bwd, sv['q'], sv['k'], sv['v'], sv['gb'], sv['tm'], sv['s'], do)
    conv = p['conv_qkv']
    dhq_q, dcw_q = _qkv_pre_bwd(sv['hq'], conv, dq, 0, 4, True, HEAD_A ** -0.5, "l0_q_pre_bwd")
    dhq_k, dcw_k = _qkv_pre_bwd(sv['hq'], conv, dk, 4, 4, True, 1.0, "l0_k_pre_bwd")
    dhq_v, dcw_v = _qkv_pre_bwd(sv['hq'], conv, dv, 8, 4, False, 1.0, "l0_v_pre_bwd")
    gr['conv_qkv_a'] = jnp.concatenate([dcw_q, dcw_k, dcw_v], axis=1)
    dhq = jnp.concatenate([dhq_q, dhq_k, dhq_v], axis=1)
    dba, da_log, ddt_bias = _gates_bwd(sv['ba'], p['arow'], p['brow'], dgb, "l0_gates_bwd")
    gr['a_log_a'], gr['dt_bias_a'] = da_log[:, 4:8], ddt_bias[:, 4:8]
    dw_qkv_t = _mm(dhq, xn, "tn", "l0_in_qkv_dw", out_dtype=BF16)
    dw_gate_t = _mm(dgate, xn, "tn", "l0_in_gate_dw", out_dtype=BF16)
    dw_ba_t = _mm(dba, xn, "tn", "l0_in_ba_dw", out_dtype=BF16)
    dw_u_t = _mm(du, xn, "tn", "l0_in_u_dw", out_dtype=BF16)
    dw_in_t = _as_pieces(jnp.concatenate([dw_qkv_t, dw_gate_t, dw_ba_t[:8], dw_u_t], axis=0))
    riders.grad('w_in_t', jnp.concatenate(
        [dw_in_t, jnp.zeros((N_DEV, dict(PIECES)['w_in_t'] - W_IN_PIECE, D_MODEL), BF16)], axis=1))
    dxn = riders.run("l0_in_qkv_dx", _mm, dhq, wts['w_qkv_t'], "nn")
    dxn = _mm(dgate, wts['w_gate_t'], "nn", "l0_in_gate_dx", res=dxn)
    dxn = _mm(dba, wts['w_ba_t'], "nn", "l0_in_ba_dx", res=dxn)
    dxn = riders.run("l0_in_u_dx", _mm, du, wts['w_u_t'], "nn", res=dxn)
    return dxn, gr


def _xa_fwd(x1, g, mem_n, wq, wkv_t, wo, tag, riders):
    xq = _rms_fwd(x1, g, BF16, tag + "_norm")
    q = _mm(xq, wq, "nn", tag + "_q", out_dtype=BF16)
    kv = _mm(mem_n, wkv_t, "nt", tag + "_kv", out_dtype=BF16)
    o = riders.run(tag + "_attn", _attn_fwd, q, kv)
    x2 = _mm(o, wo, "nn", tag + "_o", res=x1)
    return x2, dict(xq=xq, q=q, kv=kv, o=o)


def _xa_bwd(dx2, x1, g, mem_n, wq, wkv_t, wo, sv, tag, layer, riders):
    do = _mm(dx2, wo, "nt", tag + "_o_dx", out_dtype=BF16)
    riders.grad('wo%d' % layer, _as_pieces(_mm(sv['o'], dx2, "tn", tag + "_o_dw", out_dtype=BF16)))
    dq, dk, dv = _attn_bwd(sv['q'], sv['kv'], do, tag + "_attn_bwd")
    dkv = jnp.concatenate([dk, dv], axis=1).astype(BF16)
    dxq = _mm(dq, wq, "nt", tag + "_q_dx")
    riders.grad('wq%d' % layer, _as_pieces(_mm(sv['xq'], dq, "tn", tag + "_q_dw", out_dtype=BF16)))
    dmem_n = _mm(dkv, wkv_t, "nn", tag + "_kv_dx")
    riders.grad('wkv_t%d' % layer, _as_pieces(_mm(dkv, mem_n, "tn", tag + "_kv_dw", out_dtype=BF16)))
    dx1, dg = riders.run(tag + "_norm_bwd", _rms_bwd, x1, g, dxq, dx2)
    return dx1, dmem_n, dg


def _ffn_fwd(x2, g, w_up_t, conv, w_down, tag, riders):
    xf = _rms_fwd(x2, g, BF16, tag + "_norm")
    h = riders.run(tag + "_up", _mm, xf, w_up_t, "nt")
    a = riders.run(tag + "_act", _ffn_act_fwd, h, conv)
    x3 = _mm(a, w_down, "nn", tag + "_down", res=x2)
    return x3, dict(xf=xf, h=h, a=a)


def _ffn_bwd(dx3, x2, g, w_up_t, conv, w_down, sv, tag, layer, riders):
    da = _mm(dx3, w_down, "nt", tag + "_down_dx")
    riders.grad('down%d' % layer, _as_pieces(_mm(sv['a'], dx3, "tn", tag + "_down_dw", out_dtype=BF16)))
    dh_gate, dh_val, dconv = riders.run(tag + "_act_bwd", _ffn_act_bwd, sv['h'], conv, da)
    dxf = _mm(dh_gate, w_up_t, "nn", tag + "_up_dx_gate")
    dxf = riders.run(tag + "_up_dx", _mm, dh_val, w_up_t, "nn", res=dxf, b_k_part=1)
    rows = 2 * D_FF
    dw_up_t = _mm(dh_gate, sv['xf'], "tn", tag + "_up_dw_gate", out_dtype=BF16, into=(rows, 0))
    dw_up_t = riders.run(tag + "_up_dw", _mm, dh_val, sv['xf'], "tn", out_dtype=BF16, into=(rows, 1, dw_up_t))
    riders.grad('up_t%d' % layer, _as_pieces(dw_up_t))
    dx2, dg = riders.run(tag + "_norm_bwd", _rms_bwd, x2, g, dxf, dx3)
    return dx2, dconv, dg


BIG_NAMES, SMALL_NAMES, REP_NAMES = list(BIG_SHARDED), list(SMALL_SHARDED), list(REPLICATED)
SMALL_SIZES = [int(np.prod(_shard_shape(*SMALL_SHARDED[n]))) for n in SMALL_NAMES]


PIECES = [('w_in_t', 384), ('w_glu', 32), ('w_out', 128), ('pool_w', 32), ('wq0', 128), ('wq1', 128),
          ('wkv_t0', 256), ('wkv_t1', 256), ('wo0', 128), ('wo1', 128), ('up_t0', 704), ('up_t1', 704),
          ('down0', 352), ('down1', 352)]
W_IN_ROWS = 4 * WIDTH_A + 2 * N_HEADS_A + SSM_WIDTH
W_IN_PIECE = W_IN_ROWS // N_DEV


def _row_tile(rows):
    return max(t for t in range(16, min(rows, 512) + 1, 16) if rows % t == 0)


class _Riders:
    def __init__(self):
        self.waiting = {}
        self.deferred = {}
        self.grads = {}
        self.groups = []
        self.reduced = {}

    def add(self, host, comm, then):
        self.waiting.setdefault(host, []).append((comm, then))

    def after(self, marker, then):
        self.deferred.setdefault(marker, []).append(then)

    def mark(self, name, out=None):
        for cont in self.deferred.pop(name, []):
            step = cont()
            if step is not None:
                values, then = step
                out, values = lax.optimization_barrier((out, values))
                then(values)
        return out

    def run(self, name, fn, *args, **kw):
        riders = self.waiting.pop(name, [])
        if not riders:
            out = fn(*args, name=name, **kw)
        else:
            out, couts = fn(*args, name=name, comm=[c for c, _ in riders], **kw)
            for (_, then), got in zip(riders, couts):
                then(got)
        return self.mark(name, out)

    def grad(self, key, pieces):
        self.grads[key] = pieces
        for group in [g for g in self.groups if all(k in self.grads for k in g[1])]:
            self.groups.remove(group)
            self._reduce(*group)

    def _reduce(self, name, keys, pair_marker, sum_marker):
        arrays = [self.grads[k] for k in keys]
        rows = sum(a.shape[1] for a in arrays)
        tile = _row_tile(rows)
        from_sibling = _sequencer_swap(arrays, name + "_to_sibling")

        def after_swap():
            if len(arrays) == 1:
                chip_sums = _pair_sum_pieces(arrays[0], from_sibling, name + "_pair_sum", tr=tile)
            else:
                core = lax.axis_index("c")
                keep = jnp.concatenate(
                    [lax.dynamic_index_in_dim(a.reshape(4, 2, a.shape[1], PACK_COLS), core, 1, keepdims=False)
                     for a in arrays], axis=1)
                chip_sums = _pair_sum(keep, from_sibling, name + "_pair_sum", tr=tile)

            def exchange_among_chips(chip_sums):
                from_chips = _sequencer_chips(chip_sums, name + "_to_chips")

                def store(total):
                    off = 0
                    for k, a in zip(keys, arrays):
                        self.reduced[k] = total[off:off + a.shape[1]]
                        off += a.shape[1]

                self.after(sum_marker, lambda: (
                    _chip_sum(chip_sums, from_chips, name + "_chip_sum", tr=tile), store))

            return chip_sums, exchange_among_chips

        self.after(pair_marker, after_swap)


class _Weights:
    def __init__(self, inp):
        bf = lambda a: a.astype(BF16)
        local = {'w_in_t': bf(inp['w_in_ab'][0]).T, 'w_glu': bf(inp['w_glu_b'][0]), 'w_out': bf(inp['w_out_ab'][0]),
                 'pool_w': bf(inp['pool_w'][0]),
                 'small': _pack([inp[n] for n in SMALL_NAMES])}
        for l in range(2):
            local['wq%d' % l] = bf(inp['xa_wq'][l])
            local['wkv_t%d' % l] = bf(inp['xa_wkv'][l]).T
            local['wo%d' % l] = bf(inp['xa_wo'][l])
            local['up_t%d' % l] = bf(inp['ffn_w_up'][l]).T
            local['down%d' % l] = bf(inp['ffn_w_down'][l])
        self.local, self.full = local, {}

    def plan(self, keys):
        return _gather_comm([self.local[k] for k in keys])

    def gather_by_sequencer(self, keys, after, name, collective_id):
        arrays = [self.local[k] for k in keys]
        tie = (after.reshape(-1)[0] * 0.0).astype(arrays[0].dtype)
        arrays[0] = arrays[0] + tie
        self.land(keys, _sequencer_gather(arrays, name, collective_id))

    def land(self, keys, gathered):
        for k, g in zip(keys, gathered):
            if k == 'small':
                off = 0
                for n, size in zip(SMALL_NAMES, SMALL_SIZES):
                    self.full[n] = _merge_shards(g.reshape(N_DEV, -1)[:, off:off + size], *SMALL_SHARDED[n])
                    off += size
            elif k == 'pool_w':
                self.full[k] = jnp.swapaxes(g, 0, 1).reshape(len(POOL_WINDOWS), POOL_GROUP, POOL_GROUP)
            else:
                self.full[k] = g.reshape(N_DEV * g.shape[1], g.shape[2])


GATHER_FIRST = ['w_in_t', 'small']
GATHER_LAYER0 = ['w_glu', 'w_out', 'wq0', 'wkv_t0', 'wo0', 'down0', 'up_t0']
GATHER_LAYER1 = ['pool_w', 'wq1', 'wkv_t1', 'wo1', 'up_t1', 'down1']
GATHER_LAYER0_ID, GATHER_LAYER1_ID = 7, 8
GRAD_RIDES = [('g_down1', ['down1'], 'l1_ffn_act_bwd', 'l1_xa_norm_bwd'),
              ('g_up1', ['up_t1'], 'l1_xa_norm_bwd', 'l0_ffn_up_dx'),
              ('g_xa1', ['wq1', 'wkv_t1', 'wo1', 'pool_w'], 'l0_ffn_act_bwd', 'l0_xa_norm_bwd'),
              ('g_down0', ['down0'], 'l0_ffn_up_dx', 'l0_s5_scan_bwd'),
              ('g_l0', ['up_t0', 'wq0', 'wkv_t0', 'wo0'], 'l0_s5_cx_dx', 'l0_in_u_dx'),
              ('g_out', ['w_out', 'w_glu'], 'l0_s5_scan_bwd', 'l0_in_u_dx'),
              ('g_in', ['w_in_t'], 'l0_in_u_dx', 'adamw_pool_w')]


def _local_step(inp):
    f32_of = lambda n: inp[n].astype(F32)
    weights = _Weights(inp)
    riders = _Riders()
    riders.groups = list(GRAD_RIDES)
    full = weights.full
    weights.land(GATHER_FIRST, _comm_only(weights.plan(GATHER_FIRST), "gather_first"))
    weights.gather_by_sequencer(GATHER_LAYER0, full['w_in_t'], "gather_layer0", GATHER_LAYER0_ID)
    w_in_t = full['w_in_t']
    wts0 = dict(w_qkv_t=w_in_t[:3 * WIDTH_A], w_gate_t=w_in_t[3 * WIDTH_A:4 * WIDTH_A],
                w_ba_t=jnp.concatenate([w_in_t[4 * WIDTH_A:4 * WIDTH_A + 8], jnp.zeros((LANE - 8, D_MODEL), BF16)], 0),
                w_u_t=w_in_t[4 * WIDTH_A + 8:])
    lb_disc, disc_vjp = jax.vjp(_s5_discretise, f32_of('ssm_lambda_re')[0], f32_of('ssm_lambda_im')[0],
                                f32_of('ssm_b_re')[0], f32_of('ssm_b_im')[0], f32_of('ssm_log_dt')[0])
    b_in, c_out, a_row = _s5_matrices(*lb_disc, f32_of('ssm_c_re')[0], f32_of('ssm_c_im')[0])
    zeros4 = jnp.zeros((1, 4), F32)
    p0 = dict(conv_qkv=full['conv_qkv_a'][0], onorm_g=f32_of('onorm_g_a'),
              arow=jnp.concatenate([zeros4, f32_of('a_log_a'), jnp.zeros((1, LANE - 8), F32)], 1),
              brow=jnp.concatenate([zeros4, f32_of('dt_bias_a'), jnp.zeros((1, LANE - 8), F32)], 1),
              b_in=b_in.astype(BF16), c_out=c_out.astype(BF16), a_row=a_row,
              d_row=f32_of('ssm_d').reshape(1, SSM_WIDTH), b_glu=f32_of('b_glu_b'))

    x0 = inp['x'][0]
    mem_n = _rms_fwd(inp['mem'][0], inp['norm_mem_g'], BF16, "mem_norm")
    xn0 = _rms_fwd(x0, inp['norm_mix_g'][0], BF16, "l0_mix_norm")
    x1, sv_mix0 = _hybrid_fwd(xn0, x0, wts0, p0, weights, riders)
    x2, sv_xa0 = _xa_fwd(x1, inp['norm_xa_g'][0], mem_n, full['wq0'], full['wkv_t0'], full['wo0'], "l0_xa", riders)
    x3, sv_ffn0 = _ffn_fwd(x2, inp['norm_ffn_g'][0], full['up_t0'], full['ffn_conv'][0], full['down0'], "l0_ffn", riders)
    xn1 = _rms_fwd(x3, inp['norm_mix_g'][1], F32, "l1_mix_norm")
    x4 = _pool_fwd(xn1, full['pool_w'], full['pool_scale'], x3, "l1_pool")
    x5, sv_xa1 = _xa_fwd(x4, inp['norm_xa_g'][1], mem_n, full['wq1'], full['wkv_t1'], full['wo1'], "l1_xa", riders)
    x6, sv_ffn1 = _ffn_fwd(x5, inp['norm_ffn_g'][1], full['up_t1'], full['ffn_conv'][1], full['down1'], "l1_ffn", riders)
    loss_part, dx6, dg_final = _loss_head(x6, inp['norm_final_g'], inp['loss_target'][0], "loss_head")

    dx5, dconv1, dg_ffn1 = _ffn_bwd(dx6, x5, inp['norm_ffn_g'][1], full['up_t1'], full['ffn_conv'][1], full['down1'],
                                    sv_ffn1, "l1_ffn", 1, riders)
    dx4, dmem1, dg_xa1 = _xa_bwd(dx5, x4, inp['norm_xa_g'][1], mem_n, full['wq1'], full['wkv_t1'], full['wo1'],
                                 sv_xa1, "l1_xa", 1, riders)
    dxn1, dpool_w, dpool_scale = _pool_bwd(xn1, full['pool_w'], full['pool_scale'], dx4, "l1_pool_bwd")
    pool_pieces = jnp.swapaxes(dpool_w.astype(BF16).reshape(len(POOL_WINDOWS), N_DEV, -1, POOL_GROUP), 0, 1)
    riders.grad('pool_w', pool_pieces.reshape(N_DEV, -1, PACK_COLS))
    dx3, dg_mix1 = riders.run("l1_mix_norm_bwd", _rms_bwd, x3, inp['norm_mix_g'][1], dxn1, dx4)
    dx2, dconv0, dg_ffn0 = _ffn_bwd(dx3, x2, inp['norm_ffn_g'][0], full['up_t0'], full['ffn_conv'][0], full['down0'],
                                    sv_ffn0, "l0_ffn", 0, riders)
    dx1, dmem0, dg_xa0 = _xa_bwd(dx2, x1, inp['norm_xa_g'][0], mem_n, full['wq0'], full['wkv_t0'], full['wo0'],
                                 sv_xa0, "l0_xa", 0, riders)
    dxn0, g_mix0 = _hybrid_bwd(dx1, xn0, wts0, p0, sv_mix0, riders)
    grad_x, dg_mix0 = _rms_bwd(x0, inp['norm_mix_g'][0], dxn0, dx1, "l0_mix_norm_bwd")
    _, dg_mem = _rms_bwd(inp['mem'][0], inp['norm_mem_g'], dmem0 + dmem1, None, "mem_norm_bwd")
    assert not riders.groups and not riders.waiting and all(k.startswith("adamw_") for k in riders.deferred), (
        riders.groups, list(riders.waiting), list(riders.deferred))

    db_in, dc_out, da_row, dd = g_mix0['s5']
    dlb_re, dlb_im, dbb_re, dbb_im, dc_re, dc_im = _s5_matrix_grads(db_in, dc_out, da_row)
    dlam_re, dlam_im, dbr, dbi, dlog_dt = disc_vjp((dlb_re, dlb_im, dbb_re, dbb_im))

    rep_grads = {
        'norm_mix_g': jnp.concatenate([dg_mix0, dg_mix1], 0), 'norm_xa_g': jnp.concatenate([dg_xa0, dg_xa1], 0),
        'norm_ffn_g': jnp.concatenate([dg_ffn0, dg_ffn1], 0), 'norm_mem_g': dg_mem.reshape(-1),
        'norm_final_g': dg_final.reshape(-1), 'a_log_a': g_mix0['a_log_a'], 'dt_bias_a': g_mix0['dt_bias_a'],
        'onorm_g_a': g_mix0['onorm_g_a'], 'ssm_lambda_re': dlam_re[None], 'ssm_lambda_im': dlam_im[None],
        'ssm_b_re': dbr[None], 'ssm_b_im': dbi[None], 'ssm_c_re': dc_re[None], 'ssm_c_im': dc_im[None],
        'ssm_d': dd.reshape(1, N_GROUPS, SSM_GROUP), 'ssm_log_dt': dlog_dt[None], 'b_glu_b': g_mix0['b_glu_b']}
    small_grads = {'conv_qkv_a': g_mix0['conv_qkv_a'][None], 'pool_scale': dpool_scale,
                   'ffn_conv': jnp.stack([dconv0, dconv1])}
    return loss_part, grad_x, riders, rep_grads, small_grads


ADAMW_ORDER = ['ffn_w_up', 'ffn_w_down', 'xa_wkv', 'xa_wq', 'xa_wo', 'w_out_ab', 'w_glu_b', 'pool_w', 'w_in_ab']


def _update(inp, loss_part, grad_x, riders, rep_grads, small_grads):
    dev = _device_index()
    misc_local = _pack([rep_grads[n] for n in REP_NAMES] + [small_grads[n] for n in SMALL_NAMES] + [loss_part])
    (misc_all,) = _sequencer_gather([misc_local], "gather_small_grads", GATHER_LAYER0_ID)
    piece = lambda key: riders.reduced[key]
    both = lambda name: jnp.stack([piece(name + '0'), piece(name + '1')])
    swap = lambda a: jnp.swapaxes(a, -1, -2)
    reduced = {'w_in_ab': lambda: piece('w_in_t')[:W_IN_PIECE][None],
               'w_glu_b': lambda: piece('w_glu').reshape(inp['w_glu_b'].shape),
               'w_out_ab': lambda: piece('w_out')[None], 'pool_w': lambda: piece('pool_w').reshape(inp['pool_w'].shape),
               'xa_wq': lambda: both('wq'), 'xa_wkv': lambda: both('wkv_t'), 'xa_wo': lambda: both('wo'),
               'ffn_w_up': lambda: both('up_t'), 'ffn_w_down': lambda: both('down')}
    transposed = ('w_in_ab', 'xa_wkv', 'ffn_w_up')
    grads, upd = {}, {}
    assert sorted(ADAMW_ORDER) == sorted(BIG_NAMES)
    for n in ADAMW_ORDER:
        fix = swap if n in transposed else (lambda a: a)
        g = reduced[n]()
        out = riders.run("adamw_" + n, _adamw, fix(inp[n]), g, fix(inp['m_' + n]), fix(inp['v_' + n]))
        upd[n], grads[n] = tuple(fix(o) for o in out), fix(g)
    assert not riders.waiting and not riders.deferred, (list(riders.waiting), list(riders.deferred))
    misc_sum = _sum_leading(misc_all, "small_grads_sum")
    misc = _unpack(misc_sum, [inp[n].shape for n in REP_NAMES] + [SMALL_SHARDED[n][0] for n in SMALL_NAMES] + [()])
    loss = misc.pop()
    for n, g in zip(REP_NAMES, misc):
        grads[n] = g
    for n, g in zip(SMALL_NAMES, misc[len(REP_NAMES):]):
        grads[n] = lax.dynamic_index_in_dim(_split_shards(g, SMALL_SHARDED[n][1]), dev, 0, keepdims=False
                                            ).reshape(inp[n].shape)
    tiny_names = REP_NAMES + SMALL_NAMES
    rep_total = sum(int(np.prod(inp[n].shape)) for n in REP_NAMES)
    packs = [_pack([inp[prefix + n] for n in tiny_names]) for prefix in ('', 'm_', 'v_')]
    g_pack = _pack([misc_sum.reshape(-1)[:rep_total]] + [grads[n] for n in SMALL_NAMES])
    tiny_out = [_unpack(o, [inp[n].shape for n in tiny_names])
                for o in _adamw(packs[0], g_pack, packs[1], packs[2], "adamw_small")]
    for i, n in enumerate(tiny_names):
        upd[n] = tuple(o[i] for o in tiny_out)

    outs = [loss, grad_x[None]]
    outs += [grads[n] for n in WEIGHT_NAMES]
    for i in range(3):
        outs += [upd[n][i] for n in WEIGHT_NAMES]
    return tuple(outs)


def _step(inp):
    loss_part, grad_x, riders, rep_grads, small_grads = _local_step(inp)
    return _update(inp, loss_part, grad_x, riders, rep_grads, small_grads)


INPUT_NAMES = (['x', 'mem'] + WEIGHT_NAMES + ['loss_target'] + ['m_' + n for n in WEIGHT_NAMES]
               + ['v_' + n for n in WEIGHT_NAMES])


def kernel(x, mem, norm_mix_g, norm_xa_g, norm_ffn_g, norm_mem_g, norm_final_g, w_in_ab, conv_qkv_a, a_log_a, dt_bias_a, onorm_g_a, ssm_lambda_re, ssm_lambda_im, ssm_b_re, ssm_b_im, ssm_c_re, ssm_c_im, ssm_d, ssm_log_dt, w_glu_b, b_glu_b, w_out_ab, pool_w, pool_scale, xa_wq, xa_wkv, xa_wo, ffn_w_up, ffn_conv, ffn_w_down, loss_target, m_norm_mix_g, m_norm_xa_g, m_norm_ffn_g, m_norm_mem_g, m_norm_final_g, m_w_in_ab, m_conv_qkv_a, m_a_log_a, m_dt_bias_a, m_onorm_g_a, m_ssm_lambda_re, m_ssm_lambda_im, m_ssm_b_re, m_ssm_b_im, m_ssm_c_re, m_ssm_c_im, m_ssm_d, m_ssm_log_dt, m_w_glu_b, m_b_glu_b, m_w_out_ab, m_pool_w, m_pool_scale, m_xa_wq, m_xa_wkv, m_xa_wo, m_ffn_w_up, m_ffn_conv, m_ffn_w_down, v_norm_mix_g, v_norm_xa_g, v_norm_ffn_g, v_norm_mem_g, v_norm_final_g, v_w_in_ab, v_conv_qkv_a, v_a_log_a, v_dt_bias_a, v_onorm_g_a, v_ssm_lambda_re, v_ssm_lambda_im, v_ssm_b_re, v_ssm_b_im, v_ssm_c_re, v_ssm_c_im, v_ssm_d, v_ssm_log_dt, v_w_glu_b, v_b_glu_b, v_w_out_ab, v_pool_w, v_pool_scale, v_xa_wq, v_xa_wkv, v_xa_wo, v_ffn_w_up, v_ffn_conv, v_ffn_w_down):
    args = (x, mem, norm_mix_g, norm_xa_g, norm_ffn_g, norm_mem_g, norm_final_g, w_in_ab, conv_qkv_a, a_log_a, dt_bias_a, onorm_g_a, ssm_lambda_re, ssm_lambda_im, ssm_b_re, ssm_b_im, ssm_c_re, ssm_c_im, ssm_d, ssm_log_dt, w_glu_b, b_glu_b, w_out_ab, pool_w, pool_scale, xa_wq, xa_wkv, xa_wo, ffn_w_up, ffn_conv, ffn_w_down, loss_target, m_norm_mix_g, m_norm_xa_g, m_norm_ffn_g, m_norm_mem_g, m_norm_final_g, m_w_in_ab, m_conv_qkv_a, m_a_log_a, m_dt_bias_a, m_onorm_g_a, m_ssm_lambda_re, m_ssm_lambda_im, m_ssm_b_re, m_ssm_b_im, m_ssm_c_re, m_ssm_c_im, m_ssm_d, m_ssm_log_dt, m_w_glu_b, m_b_glu_b, m_w_out_ab, m_pool_w, m_pool_scale, m_xa_wq, m_xa_wkv, m_xa_wo, m_ffn_w_up, m_ffn_conv, m_ffn_w_down, v_norm_mix_g, v_norm_xa_g, v_norm_ffn_g, v_norm_mem_g, v_norm_final_g, v_w_in_ab, v_conv_qkv_a, v_a_log_a, v_dt_bias_a, v_onorm_g_a, v_ssm_lambda_re, v_ssm_lambda_im, v_ssm_b_re, v_ssm_b_im, v_ssm_c_re, v_ssm_c_im, v_ssm_d, v_ssm_log_dt, v_w_glu_b, v_b_glu_b, v_w_out_ab, v_pool_w, v_pool_scale, v_xa_wq, v_xa_wkv, v_xa_wo, v_ffn_w_up, v_ffn_conv, v_ffn_w_down)
    return _step(dict(zip(INPUT_NAMES, args)))
```

```python
import functools
import math

import numpy as np
import jax
import jax.numpy as jnp
from jax import lax
from jax.experimental import pallas as pl
from jax.experimental.pallas import tpu as pltpu
from jax.experimental.pallas import tpu_sc as plsc

F32, BF16 = jnp.float32, jnp.bfloat16
HIGH, HIGHEST = lax.Precision.HIGH, lax.Precision.HIGHEST
MESH = pl.DeviceIdType.MESH

N_DEV = 8
SEQ, D_MODEL, MEM_LEN = 2048, 1024, 256
WIDTH_A, N_HEADS_A, HEAD_A, CONV_A = 512, 4, 128, 4
GDR_CHUNK = 128
SSM_WIDTH, SSM_GROUP, N_GROUPS, SSM_STATE = 512, 16, 32, 64
SSM_CH = N_GROUPS * SSM_STATE
SCAN_CB = 512
POOL_WINDOWS = (2, 4, 8, 16)
POOL_GROUP = 256
N_HEADS_X, HEAD_X = 4, 256
D_FF, CONV_FFN = 2816, 3
RMS_EPS = 1e-6
ADAM_LR, ADAM_B1, ADAM_B2, ADAM_EPS, ADAM_WD, ADAM_STEP = 0.001, 0.9, 0.999, 1e-08, 0.01, 10
LANE = 128
PACK_COLS = 1024
VMEM_LIMIT_BYTES = 56 * 1024 * 1024


def _params(sem=None):
    return pltpu.CompilerParams(dimension_semantics=sem, vmem_limit_bytes=VMEM_LIMIT_BYTES)


class Comm:
    def __init__(self, inputs, out_shapes, sems, start, end, mid=None):
        self.inputs, self.out_shapes, self.sems = list(inputs), list(out_shapes), list(sems)
        self.start, self.mid, self.end = start, mid, end


def _merge_comms(comms):
    comms = [c for c in comms if c is not None]
    if not comms:
        return None, []
    bounds, ni, no, ns = [], 0, 0, 0
    for c in comms:
        bounds.append((ni, no, ns))
        ni, no, ns = ni + len(c.inputs), no + len(c.out_shapes), ns + len(c.sems)

    def phase(which):
        def run(ins, outs, sems):
            for c, (i0, o0, s0) in zip(comms, bounds):
                fn = getattr(c, which)
                if fn is not None:
                    fn(ins[i0:i0 + len(c.inputs)], outs[o0:o0 + len(c.out_shapes)], sems[s0:s0 + len(c.sems)])
        return run

    merged = Comm([a for c in comms for a in c.inputs], [s for c in comms for s in c.out_shapes],
                  [s for c in comms for s in c.sems], phase("start"), phase("end"), phase("mid"))
    return merged, [(o0, o0 + len(c.out_shapes)) for c, (_, o0, _) in zip(comms, bounds)]


def _call(body, *, name, grid, in_specs, out_specs, out_shape, args, scratch_shapes=(), sem=None, comm=None):
    single = not isinstance(out_shape, (list, tuple))
    out_specs_l = [out_specs] if single else list(out_specs)
    out_shape_l = [out_shape] if single else list(out_shape)
    scratch_shapes = list(scratch_shapes)
    merged, spans = _merge_comms(comm if isinstance(comm, (list, tuple)) else [comm])
    if merged is None:
        outs = pl.pallas_call(body, name=name, grid=grid, in_specs=list(in_specs), out_specs=out_specs_l,
                              out_shape=out_shape_l, scratch_shapes=scratch_shapes, compiler_params=_params(sem))(*args)
        outs = outs[0] if single else outs
        return outs if comm is None else (outs, [])
    n_in, n_out, n_scr = len(in_specs), len(out_specs_l), len(scratch_shapes)
    ci, co = len(merged.inputs), len(merged.out_shapes)
    total = int(np.prod(grid))

    def wrapped(*refs):
        ins, cins = refs[:n_in], refs[n_in:n_in + ci]
        outs, couts = refs[n_in + ci:n_in + ci + n_out], refs[n_in + ci + n_out:n_in + ci + n_out + co]
        scr, csems = refs[n_in + ci + n_out + co:n_in + ci + n_out + co + n_scr], refs[n_in + ci + n_out + co + n_scr:]
        lin = pl.program_id(0)
        for d in range(1, len(grid)):
            lin = lin * grid[d] + pl.program_id(d)
        pl.when(lin == 0)(lambda: merged.start(cins, couts, csems))
        body(*ins, *outs, *scr)
        mid_step = min((3 * total) // 4, total - 1)
        pl.when(lin == mid_step)(lambda: merged.mid(cins, couts, csems))
        pl.when(lin == total - 1)(lambda: merged.end(cins, couts, csems))

    any_spec = pl.BlockSpec(memory_space=pl.ANY)
    res = pl.pallas_call(
        wrapped, name=name, grid=grid, in_specs=list(in_specs) + [any_spec] * ci,
        out_specs=out_specs_l + [any_spec] * co, out_shape=out_shape_l + merged.out_shapes,
        scratch_shapes=scratch_shapes + merged.sems,
        compiler_params=_params(("arbitrary",) * len(grid)))(*args, *merged.inputs)
    outs, couts = res[:n_out], res[n_out:]
    return (outs[0] if single else list(outs)), [list(couts[a:b]) for a, b in spans]


def _comm_only(comm, name):
    def body():
        pass

    _, couts = _call(body, name=name, grid=(1,), in_specs=[], out_specs=[], out_shape=[], args=[], comm=comm)
    return couts[0]


def _tile(dim, pref):
    best = None
    for t in range(LANE, min(dim, pref) + 1, LANE):
        if dim % t == 0:
            best = t
    return best if best is not None else dim


MM_VMEM_BUDGET = 40 * 1024 * 1024


def _mm_tiles(m, n, k, a_bytes, b_bytes, o_bytes, r_bytes):
    for tk in (k, _tile(k, 2048), _tile(k, 1024), _tile(k, 512)):
        for tm, tn in ((1024, 1536), (1024, 1024), (1024, 512), (512, 512), (256, 512), (256, 256)):
            tm, tn = _tile(m, tm), _tile(n, tn)
            acc = 0 if tk == k else tm * tn * 4
            need = 2 * (tm * tk * a_bytes + tk * tn * b_bytes + tm * tn * (o_bytes + r_bytes)) + acc
            if need <= MM_VMEM_BUDGET:
                return tm, tn, tk
    raise ValueError("no matmul tiling fits VMEM")


def _mm(a, b, mode, name, out_dtype=F32, res=None, comm=None):
    if mode == "nn":
        (m, k), n = a.shape, b.shape[1]
    elif mode == "nt":
        (m, k), n = a.shape, b.shape[0]
    else:
        (k, m), n = a.shape, b.shape[1]
    tm, tn, tk = _mm_tiles(m, n, k, a.dtype.itemsize, b.dtype.itemsize, jnp.dtype(out_dtype).itemsize,
                           0 if res is None else res.dtype.itemsize)
    nk = k // tk
    dims = {"nn": ((1,), (0,)), "nt": ((1,), (1,)), "tn": ((0,), (0,))}[mode]

    def body(*refs):
        if res is None:
            a_ref, b_ref, o_ref = refs[:3]
            r_ref = None
        else:
            a_ref, b_ref, r_ref, o_ref = refs[:4]
        part = lax.dot_general(a_ref[...].astype(BF16), b_ref[...].astype(BF16), (dims, ((), ())),
                               preferred_element_type=F32)

        def finish(out):
            if r_ref is not None:
                out = out + r_ref[...].astype(F32)
            o_ref[...] = out.astype(out_dtype)

        if nk == 1:
            finish(part)
            return
        acc = refs[-1]
        kk = pl.program_id(2)

        @pl.when(kk == 0)
        def _():
            acc[...] = part

        @pl.when(kk > 0)
        def _():
            acc[...] += part

        @pl.when(kk == nk - 1)
        def _():
            finish(acc[...])

    a_spec = (pl.BlockSpec((tk, tm), lambda i, j, q: (q, i)) if mode == "tn"
              else pl.BlockSpec((tm, tk), lambda i, j, q: (i, q)))
    b_spec = (pl.BlockSpec((tn, tk), lambda i, j, q: (j, q)) if mode == "nt"
              else pl.BlockSpec((tk, tn), lambda i, j, q: (q, j)))
    o_spec = pl.BlockSpec((tm, tn), lambda i, j, q: (i, j))
    in_specs, args = [a_spec, b_spec], [a, b]
    if res is not None:
        in_specs.append(o_spec)
        args.append(res)
    return _call(body, name=name, grid=(m // tm, n // tn, nk), in_specs=in_specs, out_specs=o_spec,
                 out_shape=jax.ShapeDtypeStruct((m, n), out_dtype),
                 scratch_shapes=[] if nk == 1 else [pltpu.VMEM((tm, tn), F32)],
                 sem=("parallel", "parallel", "arbitrary"), args=args, comm=comm)


def _mm_bd(a, b, mode, name, out_dtype=F32, res=None, comm=None, tm=1024):
    if mode == "tn":
        k = a.shape[0]
        nb = min(a.shape[1], b.shape[1]) // LANE
        ma, n = a.shape[1] // nb, b.shape[1] // nb

        def body(a_ref, b_ref, o_ref):
            o_ref[0] = lax.dot_general(a_ref[...].astype(BF16), b_ref[...].astype(BF16), (((0,), (0,)), ((), ())),
                                       preferred_element_type=F32).astype(out_dtype)

        return _call(body, name=name, grid=(nb,),
                     in_specs=[pl.BlockSpec((k, ma), lambda j: (0, j)), pl.BlockSpec((k, n), lambda j: (0, j))],
                     out_specs=pl.BlockSpec((1, ma, n), lambda j: (j, 0, 0)),
                     out_shape=jax.ShapeDtypeStruct((nb, ma, n), out_dtype), sem=("parallel",), args=(a, b), comm=comm)
    m = a.shape[0]
    nb = b.shape[0]
    ka = a.shape[1] // nb
    n = b.shape[2] if mode == "nn" else b.shape[1]
    tm = _tile(m, tm)
    dims = ((1,), (0,)) if mode == "nn" else ((1,), (1,))

    def body(*refs):
        if res is None:
            a_ref, b_ref, o_ref = refs
            r_ref = None
        else:
            a_ref, b_ref, r_ref, o_ref = refs
        out = lax.dot_general(a_ref[...].astype(BF16), b_ref[0].astype(BF16), (dims, ((), ())),
                              preferred_element_type=F32)
        if r_ref is not None:
            out = out + r_ref[...].astype(F32)
        o_ref[...] = out.astype(out_dtype)

    o_spec = pl.BlockSpec((tm, n), lambda i, j: (i, j))
    in_specs = [pl.BlockSpec((tm, ka), lambda i, j: (i, j)), pl.BlockSpec((1,) + b.shape[1:], lambda i, j: (j, 0, 0))]
    args = [a, b]
    if res is not None:
        in_specs.append(o_spec)
        args.append(res)
    return _call(body, name=name, grid=(m // tm, nb), in_specs=in_specs, out_specs=o_spec,
                 out_shape=jax.ShapeDtypeStruct((m, nb * n), out_dtype), sem=("parallel", "parallel"),
                 args=args, comm=comm)


def _rms_fwd(x, g, out_dtype, name, tr=256):
    rows, d = x.shape

    def body(x_ref, g_ref, o_ref):
        xv = x_ref[...]
        r = lax.rsqrt(jnp.mean(xv * xv, axis=-1, keepdims=True) + RMS_EPS)
        o_ref[...] = (xv * r * g_ref[...]).astype(out_dtype)

    return pl.pallas_call(
        body, name=name, grid=(rows // tr,),
        in_specs=[pl.BlockSpec((tr, d), lambda i: (i, 0)), pl.BlockSpec((1, d), lambda i: (0, 0))],
        out_specs=pl.BlockSpec((tr, d), lambda i: (i, 0)), out_shape=jax.ShapeDtypeStruct((rows, d), out_dtype),
        compiler_params=_params(("parallel",)))(x, g.reshape(1, d))


def _rms_bwd(x, g, dy, dres, name, tr=256, comm=None):
    rows, d = x.shape

    def body(*refs):
        if dres is None:
            x_ref, g_ref, dy_ref, dx_ref, dg_ref = refs
            r_ref = None
        else:
            x_ref, g_ref, dy_ref, r_ref, dx_ref, dg_ref = refs

        @pl.when(pl.program_id(0) == 0)
        def _():
            dg_ref[...] = jnp.zeros_like(dg_ref)

        xv, dyv = x_ref[...], dy_ref[...].astype(F32)
        r = lax.rsqrt(jnp.mean(xv * xv, axis=-1, keepdims=True) + RMS_EPS)
        xh = xv * r
        dyg = dyv * g_ref[...]
        dx = r * (dyg - xh * jnp.mean(dyg * xh, axis=-1, keepdims=True))
        if r_ref is not None:
            dx = dx + r_ref[...]
        dx_ref[...] = dx
        dg_ref[...] += jnp.sum(dyv * xh, axis=0, keepdims=True)

    blk = pl.BlockSpec((tr, d), lambda i: (i, 0))
    vec = pl.BlockSpec((1, d), lambda i: (0, 0))
    in_specs, args = [blk, vec, blk], [x, g.reshape(1, d), dy]
    if dres is not None:
        in_specs.append(blk)
        args.append(dres)
    return _call(
        body, name=name, grid=(rows // tr,), in_specs=in_specs, out_specs=[blk, vec],
        out_shape=[jax.ShapeDtypeStruct((rows, d), F32), jax.ShapeDtypeStruct((1, d), F32)],
        sem=("arbitrary",), args=args, comm=comm)


def _loss_head(x, g, target, name, tr=256):
    rows, d = x.shape

    def body(x_ref, g_ref, t_ref, loss_ref, dx_ref, dg_ref):
        @pl.when(pl.program_id(0) == 0)
        def _():
            dg_ref[...] = jnp.zeros_like(dg_ref)
            loss_ref[...] = jnp.zeros_like(loss_ref)

        xv = x_ref[...]
        r = lax.rsqrt(jnp.mean(xv * xv, axis=-1, keepdims=True) + RMS_EPS)
        xh = xv * r
        err = xh * g_ref[...] - t_ref[...]
        loss_ref[...] += 0.5 * jnp.sum(jnp.mean(err * err, axis=-1, keepdims=True), keepdims=True)
        dyv = err * (1.0 / d)
        dyg = dyv * g_ref[...]
        dx_ref[...] = r * (dyg - xh * jnp.mean(dyg * xh, axis=-1, keepdims=True))
        dg_ref[...] += jnp.sum(dyv * xh, axis=0, keepdims=True)

    blk = pl.BlockSpec((tr, d), lambda i: (i, 0))
    vec = pl.BlockSpec((1, d), lambda i: (0, 0))
    return pl.pallas_call(
        body, name=name, grid=(rows // tr,), in_specs=[blk, vec, blk],
        out_specs=[pl.BlockSpec((1, 1), lambda i: (0, 0)), blk, vec],
        out_shape=[jax.ShapeDtypeStruct((1, 1), F32), jax.ShapeDtypeStruct((rows, d), F32),
                   jax.ShapeDtypeStruct((1, d), F32)],
        compiler_params=_params(("arbitrary",)))(x, g.reshape(1, d), target)


def _shift_down(x, s):
    rows = lax.broadcasted_iota(jnp.int32, x.shape, 0)
    return jnp.where(rows >= s, pltpu.roll(x, s, 0), 0.0)


def _shift_up(x, s):
    n = x.shape[0]
    rows = lax.broadcasted_iota(jnp.int32, x.shape, 0)
    return jnp.where(rows < n - s, pltpu.roll(x, n - s, 0), 0.0)


def _sigmoid(x):
    return 1.0 / (1.0 + jnp.exp(-x))


def _silu_and_grad(x):
    s = _sigmoid(x)
    return x * s, s * (1.0 + x * (1.0 - s))


_GELU_C0, _GELU_C1 = math.sqrt(2.0 / math.pi), 0.044715


def _gelu_and_grad(x):
    th = jnp.tanh(_GELU_C0 * (x + _GELU_C1 * x * x * x))
    y = 0.5 * x * (1.0 + th)
    dy = 0.5 * (1.0 + th) + 0.5 * x * (1.0 - th * th) * _GELU_C0 * (1.0 + 3.0 * _GELU_C1 * x * x)
    return y, dy


def _ffn_act_fwd(h, w, name, tc=256, comm=None):
    t = h.shape[0]
    nb = D_FF // tc

    def body(hg_ref, hv_ref, wg_ref, wv_ref, a_ref):
        def conv(x, wr):
            return wr[2:3, :] * x + wr[1:2, :] * _shift_down(x, 1) + wr[0:1, :] * _shift_down(x, 2)

        cg = conv(hg_ref[...], wg_ref[...])
        cv = conv(hv_ref[...], wv_ref[...])
        a_ref[...] = (cg * _sigmoid(cg) * cv).astype(BF16)

    return _call(
        body, name=name, grid=(nb,),
        in_specs=[pl.BlockSpec((t, tc), lambda j: (0, j)), pl.BlockSpec((t, tc), lambda j: (0, j + nb)),
                  pl.BlockSpec((CONV_FFN, tc), lambda j: (0, j)), pl.BlockSpec((CONV_FFN, tc), lambda j: (0, j + nb))],
        out_specs=pl.BlockSpec((t, tc), lambda j: (0, j)), out_shape=jax.ShapeDtypeStruct((t, D_FF), BF16),
        sem=("parallel",), args=(h, h, w, w), comm=comm)


def _ffn_act_bwd(h, w, da, name, tc=256, comm=None):
    t = h.shape[0]
    nb = D_FF // tc

    def body(hg_ref, hv_ref, wg_ref, wv_ref, da_ref, dhg_ref, dhv_ref, dwg_ref, dwv_ref):
        hg, hv, wg, wv = hg_ref[...], hv_ref[...], wg_ref[...], wv_ref[...]
        hg1, hg2, hv1, hv2 = _shift_down(hg, 1), _shift_down(hg, 2), _shift_down(hv, 1), _shift_down(hv, 2)
        cg = wg[2:3, :] * hg + wg[1:2, :] * hg1 + wg[0:1, :] * hg2
        cv = wv[2:3, :] * hv + wv[1:2, :] * hv1 + wv[0:1, :] * hv2
        sg, dsg = _silu_and_grad(cg)
        dav = da_ref[...].astype(F32)
        dcv = dav * sg
        dcg = dav * cv * dsg

        def conv_t(dc, wr):
            return wr[2:3, :] * dc + wr[1:2, :] * _shift_up(dc, 1) + wr[0:1, :] * _shift_up(dc, 2)

        dhg_ref[...] = conv_t(dcg, wg).astype(BF16)
        dhv_ref[...] = conv_t(dcv, wv).astype(BF16)
        dwg_ref[0:1, :] = jnp.sum(dcg * hg2, axis=0, keepdims=True)
        dwg_ref[1:2, :] = jnp.sum(dcg * hg1, axis=0, keepdims=True)
        dwg_ref[2:3, :] = jnp.sum(dcg * hg, axis=0, keepdims=True)
        dwv_ref[0:1, :] = jnp.sum(dcv * hv2, axis=0, keepdims=True)
        dwv_ref[1:2, :] = jnp.sum(dcv * hv1, axis=0, keepdims=True)
        dwv_ref[2:3, :] = jnp.sum(dcv * hv, axis=0, keepdims=True)

    big = lambda off: pl.BlockSpec((t, tc), lambda j: (0, j + off))
    small = lambda off: pl.BlockSpec((CONV_FFN, tc), lambda j: (0, j + off))
    res = _call(
        body, name=name, grid=(nb,),
        in_specs=[big(0), big(nb), small(0), small(nb), big(0)],
        out_specs=[big(0), big(0), small(0), small(0)],
        out_shape=[jax.ShapeDtypeStruct((t, D_FF), BF16), jax.ShapeDtypeStruct((t, D_FF), BF16),
                   jax.ShapeDtypeStruct((CONV_FFN, D_FF), F32), jax.ShapeDtypeStruct((CONV_FFN, D_FF), F32)],
        sem=("parallel",), args=(h, h, w, w, da), comm=comm)
    (dhg, dhv, dwg, dwv), couts = res if comm is not None else (res, None)
    out = (jnp.concatenate([dhg, dhv], axis=1), jnp.concatenate([dwg, dwv], axis=1))
    return out if comm is None else (out, couts)


def _attn_probs(q, k):
    s = lax.dot_general(q.astype(BF16), k.astype(BF16), (((1,), (1,)), ((), ())),
                        preferred_element_type=F32) * (HEAD_X ** -0.5)
    s = s - jnp.max(s, axis=-1, keepdims=True)
    p = jnp.exp(s)
    return p / jnp.sum(p, axis=-1, keepdims=True)


def _attn_fwd(q, kv, name, tq=512, comm=None):
    t = q.shape[0]

    def body(q_ref, k_ref, v_ref, o_ref):
        p = _attn_probs(q_ref[...], k_ref[...])
        o_ref[...] = jnp.dot(p.astype(BF16), v_ref[...].astype(BF16), preferred_element_type=F32).astype(BF16)

    return _call(
        body, name=name, grid=(N_HEADS_X, t // tq),
        in_specs=[pl.BlockSpec((tq, HEAD_X), lambda h, i: (i, h)),
                  pl.BlockSpec((MEM_LEN, HEAD_X), lambda h, i: (0, h)),
                  pl.BlockSpec((MEM_LEN, HEAD_X), lambda h, i: (0, h + N_HEADS_X))],
        out_specs=pl.BlockSpec((tq, HEAD_X), lambda h, i: (i, h)),
        out_shape=jax.ShapeDtypeStruct((t, N_HEADS_X * HEAD_X), BF16),
        sem=("parallel", "parallel"), args=(q, kv, kv), comm=comm)


def _attn_bwd(q, kv, do, name, tq=512):
    t = q.shape[0]

    def body(q_ref, k_ref, v_ref, do_ref, dq_ref, dk_ref, dv_ref):
        @pl.when(pl.program_id(1) == 0)
        def _():
            dk_ref[...] = jnp.zeros_like(dk_ref)
            dv_ref[...] = jnp.zeros_like(dv_ref)

        qb, kb, vb, dob = (r[...].astype(BF16) for r in (q_ref, k_ref, v_ref, do_ref))
        p = _attn_probs(qb, kb)
        dp = lax.dot_general(dob, vb, (((1,), (1,)), ((), ())), preferred_element_type=F32)
        ds = p * (dp - jnp.sum(dp * p, axis=-1, keepdims=True)) * (HEAD_X ** -0.5)
        dsb = ds.astype(BF16)
        dq_ref[...] = jnp.dot(dsb, kb, preferred_element_type=F32).astype(BF16)
        dk_ref[...] += lax.dot_general(dsb, qb, (((0,), (0,)), ((), ())), preferred_element_type=F32)
        dv_ref[...] += lax.dot_general(p.astype(BF16), dob, (((0,), (0,)), ((), ())), preferred_element_type=F32)

    qs = pl.BlockSpec((tq, HEAD_X), lambda h, i: (i, h))
    ms = pl.BlockSpec((MEM_LEN, HEAD_X), lambda h, i: (0, h))
    return pl.pallas_call(
        body, name=name, grid=(N_HEADS_X, t // tq),
        in_specs=[qs, ms, pl.BlockSpec((MEM_LEN, HEAD_X), lambda h, i: (0, h + N_HEADS_X)), qs],
        out_specs=[qs, ms, ms],
        out_shape=[jax.ShapeDtypeStruct((t, D_MODEL), BF16), jax.ShapeDtypeStruct((MEM_LEN, D_MODEL), F32),
                   jax.ShapeDtypeStruct((MEM_LEN, D_MODEL), F32)],
        compiler_params=_params(("parallel", "arbitrary")))(q, kv, kv, do)


def _pool_counts(t, win):
    pos = lax.broadcasted_iota(jnp.int32, (t, 1), 0).astype(F32) + 1.0
    return 1.0 / jnp.minimum(pos, float(win))


def _pool_delta(xv, win):
    s, step = xv, 1
    while step < win:
        s = s + _shift_down(s, step)
        step *= 2
    return s * _pool_counts(xv.shape[0], win) - xv


def _pool_delta_t(dv, win):
    s, step = dv * _pool_counts(dv.shape[0], win), 1
    while step < win:
        s = s + _shift_up(s, step)
        step *= 2
    return s - dv


def _pool_fwd(xn, w, scale, res, name):
    t = xn.shape[0]

    def make_branch(win, xn_ref, w_ref, s_ref, r_ref, o_ref):
        def branch():
            dl = _pool_delta(xn_ref[...], win)
            y = jnp.dot(dl.astype(BF16), w_ref[0], preferred_element_type=F32)
            o_ref[...] = r_ref[...] + y * s_ref[...]
        return branch

    def body(xn_ref, w_ref, s_ref, r_ref, o_ref):
        for gi, win in enumerate(POOL_WINDOWS):
            pl.when(pl.program_id(0) == gi)(make_branch(win, xn_ref, w_ref, s_ref, r_ref, o_ref))

    blk = pl.BlockSpec((t, POOL_GROUP), lambda g: (0, g))
    return pl.pallas_call(
        body, name=name, grid=(len(POOL_WINDOWS),),
        in_specs=[blk, pl.BlockSpec((1, POOL_GROUP, POOL_GROUP), lambda g: (g, 0, 0)),
                  pl.BlockSpec((1, POOL_GROUP), lambda g: (0, g)), blk],
        out_specs=blk, out_shape=jax.ShapeDtypeStruct((t, D_MODEL), F32),
        compiler_params=_params(("parallel",)))(xn, w, scale, res)


def _pool_bwd(xn, w, scale, dmix, name):
    t = xn.shape[0]

    def make_branch(win, xn_ref, w_ref, s_ref, d_ref, dxn_ref, dw_ref, ds_ref):
        def branch():
            dl = _pool_delta(xn_ref[...], win).astype(BF16)
            wv = w_ref[0]
            dm = d_ref[...]
            y = jnp.dot(dl, wv, preferred_element_type=F32)
            ds_ref[...] = jnp.sum(dm * y, axis=0, keepdims=True)
            dy = (dm * s_ref[...]).astype(BF16)
            dw_ref[0] = lax.dot_general(dl, dy, (((0,), (0,)), ((), ())), preferred_element_type=F32)
            ddl = lax.dot_general(dy, wv, (((1,), (1,)), ((), ())), preferred_element_type=F32)
            dxn_ref[...] = _pool_delta_t(ddl, win)
        return branch

    def body(*refs):
        for gi, win in enumerate(POOL_WINDOWS):
            pl.when(pl.program_id(0) == gi)(make_branch(win, *refs))

    blk = pl.BlockSpec((t, POOL_GROUP), lambda g: (0, g))
    wspec = pl.BlockSpec((1, POOL_GROUP, POOL_GROUP), lambda g: (g, 0, 0))
    vec = pl.BlockSpec((1, POOL_GROUP), lambda g: (0, g))
    return pl.pallas_call(
        body, name=name, grid=(len(POOL_WINDOWS),), in_specs=[blk, wspec, vec, blk], out_specs=[blk, wspec, vec],
        out_shape=[jax.ShapeDtypeStruct((t, D_MODEL), F32),
                   jax.ShapeDtypeStruct((len(POOL_WINDOWS), POOL_GROUP, POOL_GROUP), F32),
                   jax.ShapeDtypeStruct((1, D_MODEL), F32)],
        compiler_params=_params(("parallel",)))(xn, w, scale, dmix)


def _qkv_conv(h, wr):
    return (wr[3:4, :] * h + wr[2:3, :] * _shift_down(h, 1) + wr[1:2, :] * _shift_down(h, 2)
            + wr[0:1, :] * _shift_down(h, 3))


def _qkv_pre_fwd(h, w, col0, ncols, normalize, scale, name):
    t = h.shape[0]

    def body(h_ref, w_ref, o_ref):
        c = _qkv_conv(h_ref[...], w_ref[...])
        s = c * _sigmoid(c)
        if normalize:
            s = s * lax.rsqrt(jnp.sum(s * s, axis=-1, keepdims=True) + 1e-6) * scale
        o_ref[...] = s

    return pl.pallas_call(
        body, name=name, grid=(ncols,),
        in_specs=[pl.BlockSpec((t, HEAD_A), lambda j: (0, j + col0)), pl.BlockSpec((CONV_A, HEAD_A), lambda j: (0, j + col0))],
        out_specs=pl.BlockSpec((t, HEAD_A), lambda j: (0, j)), out_shape=jax.ShapeDtypeStruct((t, ncols * HEAD_A), F32),
        compiler_params=_params(("parallel",)))(h, w)


def _qkv_pre_bwd(h, w, dy, col0, ncols, normalize, scale, name):
    t = h.shape[0]

    def body(h_ref, w_ref, dy_ref, dh_ref, dw_ref):
        hv, wr, dyv = h_ref[...], w_ref[...], dy_ref[...]
        h1, h2, h3 = _shift_down(hv, 1), _shift_down(hv, 2), _shift_down(hv, 3)
        c = wr[3:4, :] * hv + wr[2:3, :] * h1 + wr[1:2, :] * h2 + wr[0:1, :] * h3
        s, dsilu = _silu_and_grad(c)
        if normalize:
            r = lax.rsqrt(jnp.sum(s * s, axis=-1, keepdims=True) + 1e-6)
            y = s * r
            dyv = dyv * scale
            ds = r * (dyv - y * jnp.sum(dyv * y, axis=-1, keepdims=True))
        else:
            ds = dyv
        dc = ds * dsilu
        dh = (wr[3:4, :] * dc + wr[2:3, :] * _shift_up(dc, 1) + wr[1:2, :] * _shift_up(dc, 2)
              + wr[0:1, :] * _shift_up(dc, 3))
        dh_ref[...] = dh.astype(BF16)
        dw_ref[0:1, :] = jnp.sum(dc * h3, axis=0, keepdims=True)
        dw_ref[1:2, :] = jnp.sum(dc * h2, axis=0, keepdims=True)
        dw_ref[2:3, :] = jnp.sum(dc * h1, axis=0, keepdims=True)
        dw_ref[3:4, :] = jnp.sum(dc * hv, axis=0, keepdims=True)

    return pl.pallas_call(
        body, name=name, grid=(ncols,),
        in_specs=[pl.BlockSpec((t, HEAD_A), lambda j: (0, j + col0)), pl.BlockSpec((CONV_A, HEAD_A), lambda j: (0, j + col0)),
                  pl.BlockSpec((t, HEAD_A), lambda j: (0, j))],
        out_specs=[pl.BlockSpec((t, HEAD_A), lambda j: (0, j)), pl.BlockSpec((CONV_A, HEAD_A), lambda j: (0, j))],
        out_shape=[jax.ShapeDtypeStruct((t, ncols * HEAD_A), BF16), jax.ShapeDtypeStruct((CONV_A, ncols * HEAD_A), F32)],
        compiler_params=_params(("parallel",)))(h, w, dy)


def _softplus(x):
    return jnp.maximum(x, 0.0) + jnp.log1p(jnp.exp(-jnp.abs(x)))


def _gates_fwd(ba, arow, brow, name):
    t = ba.shape[0]

    def body(x_ref, a_ref, b_ref, o_ref):
        xv = x_ref[...]
        lane = lax.broadcasted_iota(jnp.int32, xv.shape, 1)
        beta = _sigmoid(xv)
        g = -jnp.exp(a_ref[...]) * _softplus(xv + b_ref[...])
        o_ref[...] = jnp.where(lane < N_HEADS_A, beta, jnp.where(lane < 2 * N_HEADS_A, g, 0.0))

    return pl.pallas_call(body, name=name, out_shape=jax.ShapeDtypeStruct((t, LANE), F32),
                          compiler_params=_params())(ba, arow, brow)


def _gates_bwd(ba, arow, brow, dgb, name):
    t = ba.shape[0]

    def body(x_ref, a_ref, b_ref, d_ref, dx_ref, da_ref, db_ref):
        xv = x_ref[...]
        dv = d_ref[0] + d_ref[1] + d_ref[2] + d_ref[3]
        lane = lax.broadcasted_iota(jnp.int32, xv.shape, 1)
        beta = _sigmoid(xv)
        ea = jnp.exp(a_ref[...])
        z = xv + b_ref[...]
        dgv = jnp.where((lane >= N_HEADS_A) & (lane < 2 * N_HEADS_A), dv, 0.0) * (-ea)
        dz = dgv * _sigmoid(z)
        dx = jnp.where(lane < N_HEADS_A, dv * beta * (1.0 - beta), dz)
        dx_ref[...] = dx.astype(BF16)
        db_ref[...] = jnp.sum(dz, axis=0, keepdims=True)
        da_ref[...] = jnp.sum(dgv * _softplus(z), axis=0, keepdims=True)

    return pl.pallas_call(
        body, name=name,
        out_shape=[jax.ShapeDtypeStruct((t, LANE), BF16), jax.ShapeDtypeStruct((1, LANE), F32),
                   jax.ShapeDtypeStruct((1, LANE), F32)],
        compiler_params=_params())(ba, arow, brow, dgb)


def _head_gates(gates, head):
    lane = lax.broadcasted_iota(jnp.int32, gates.shape, 1)
    beta = jnp.sum(jnp.where(lane == head, gates, 0.0), axis=1, keepdims=True)
    g = jnp.sum(jnp.where(lane == head + N_HEADS_A, gates, 0.0), axis=1, keepdims=True)
    return beta, g


_B_NN, _B_NT, _B_TN = ((2,), (1,)), ((2,), (2,)), ((1,), (1,))


def _bdot(a, b, dims=_B_NN, prec=None):
    if prec is None:
        a, b = a.astype(BF16), b.astype(BF16)
    return lax.dot_general(a, b, (dims, ((0,), (0,))), precision=prec, preferred_element_type=F32)


def _heads_of(ref):
    return jnp.stack([ref[:, h * HEAD_A:(h + 1) * HEAD_A] for h in range(N_HEADS_A)])


def _all_head_gates(gates):
    pairs = [_head_gates(gates, h) for h in range(N_HEADS_A)]
    return jnp.stack([b for b, _ in pairs]), jnp.stack([g for _, g in pairs])


def _gdr_terms(k, beta, g):
    h, c = k.shape[0], GDR_CHUNK
    row = lax.broadcasted_iota(jnp.int32, (c, c), 0)
    col = lax.broadcasted_iota(jnp.int32, (c, c), 1)
    causal, strict = row >= col, row > col
    lower = jnp.broadcast_to(causal.astype(F32), (h, c, c))
    gcum = _bdot(lower, jnp.broadcast_to(g, (h, c, c)), prec=HIGHEST)
    diff = gcum - jnp.swapaxes(gcum, 1, 2)
    decay = jnp.where(causal, jnp.exp(jnp.where(causal, diff, 0.0)), 0.0)
    kb = k * beta
    return row, col, causal, strict, gcum, decay, kb, _bdot(kb, k, _B_NT)


def _unit_lower_inverses(a):
    c = a.shape[1]
    eye = (lax.broadcasted_iota(jnp.int32, (c, c), 0) == lax.broadcasted_iota(jnp.int32, (c, c), 1)).astype(F32)
    p = -a
    inv = eye + p
    step = 1
    while 2 * step < c:
        p = _bdot(p, p, prec=HIGH)
        inv = inv + _bdot(inv, p, prec=HIGH)
        step *= 2
    return inv


def _gdr_fwd(q, k, v, gates, name, comm=None):
    t = q.shape[0]
    c, nh = GDR_CHUNK, N_HEADS_A
    n = t // c

    def body(q_ref, k_ref, v_ref, gb_ref, o_ref, tm_ref, s_ref, state):
        @pl.when(pl.program_id(0) == 0)
        def _():
            state[...] = jnp.zeros_like(state)

        qv, kv, vv = _heads_of(q_ref), _heads_of(k_ref), _heads_of(v_ref)
        beta, g = _all_head_gates(gb_ref[...])
        row, col, causal, strict, gcum, decay, kb, kk = _gdr_terms(kv, beta, g)
        tm = _unit_lower_inverses(jnp.where(strict, kk * decay, 0.0))
        e = jnp.exp(gcum)
        u = _bdot(tm, vv * beta, prec=HIGH)
        w = _bdot(tm, kb * e, prec=HIGH)
        p = jnp.where(causal, _bdot(qv, kv, _B_NT) * decay, 0.0)
        s = state[...]
        s_ref[:, 0] = s
        tm_ref[:, 0] = tm
        vn = u - _bdot(w, s)
        o = _bdot(qv * e, s) + _bdot(p, vn)
        for h in range(nh):
            o_ref[:, h * HEAD_A:(h + 1) * HEAD_A] = o[h]
        glast = gcum[:, c - 1:c, :]
        state[...] = s * jnp.exp(glast) + _bdot(kv * jnp.exp(glast - gcum), vn, _B_TN)

    blk = pl.BlockSpec((c, WIDTH_A), lambda i: (i, 0))
    mat = pl.BlockSpec((nh, 1, c, c), lambda i: (0, i, 0, 0))
    return _call(
        body, name=name, grid=(n,), in_specs=[blk, blk, blk, pl.BlockSpec((c, LANE), lambda i: (i, 0))],
        out_specs=[blk, mat, mat],
        out_shape=[jax.ShapeDtypeStruct((t, WIDTH_A), F32), jax.ShapeDtypeStruct((nh, n, c, c), F32),
                   jax.ShapeDtypeStruct((nh, n, HEAD_A, HEAD_A), F32)],
        scratch_shapes=[pltpu.VMEM((nh, HEAD_A, HEAD_A), F32)], sem=("arbitrary",),
        args=(q, k, v, gates), comm=comm)


def _gdr_bwd(q, k, v, gates, tm_all, s_all, do, name, comm=None):
    t = q.shape[0]
    c, nh = GDR_CHUNK, N_HEADS_A
    n = t // c

    def body(q_ref, k_ref, v_ref, gb_ref, tm_ref, s_ref, do_ref, dq_ref, dk_ref, dv_ref, dgb_ref, dstate):
        @pl.when(pl.program_id(0) == 0)
        def _():
            dstate[...] = jnp.zeros_like(dstate)

        qv, kv, vv, dov = _heads_of(q_ref), _heads_of(k_ref), _heads_of(v_ref), _heads_of(do_ref)
        beta, g = _all_head_gates(gb_ref[...])
        tm, s, dsp = tm_ref[:, 0], s_ref[:, 0], dstate[...]
        row, col, causal, strict, gcum, decay, kb, kk = _gdr_terms(kv, beta, g)
        rowsum = lambda x: jnp.sum(x, axis=2, keepdims=True)
        e = jnp.exp(gcum)
        vb, kbe = vv * beta, kb * e
        u = _bdot(tm, vb, prec=HIGH)
        w = _bdot(tm, kbe, prec=HIGH)
        qk = _bdot(qv, kv, _B_NT)
        p = jnp.where(causal, qk * decay, 0.0)
        vn = u - _bdot(w, s)
        glast = gcum[:, c - 1:c, :]
        el = jnp.exp(glast)
        f = jnp.exp(glast - gcum)
        kd = kv * f
        qe = qv * e

        dvn = _bdot(p, dov, _B_TN) + _bdot(kd, dsp)
        dglast = el[:, :, 0:1] * jnp.sum(s * dsp, axis=(1, 2), keepdims=True)
        dkd = _bdot(vn, dsp, _B_NT)
        dk = dkd * f
        df = rowsum(dkd * kv) * f[:, :, 0:1]
        dglast = dglast + jnp.sum(df, axis=1, keepdims=True)
        dgc = -df
        dp = jnp.where(causal, _bdot(dov, vn, _B_NT), 0.0)
        dqe = _bdot(dov, s, _B_NT)
        dq = dqe * e
        de = rowsum(dqe * qv)
        dstate[...] = dsp * el + _bdot(qe, dov, _B_TN) - _bdot(w, dvn, _B_TN)
        dw = -_bdot(dvn, s, _B_NT)
        dvb = _bdot(tm, dvn, _B_TN, prec=HIGH)
        dkbe = _bdot(tm, dw, _B_TN, prec=HIGH)
        da = -jnp.where(strict, _bdot(dvb, u, _B_NT) + _bdot(dkbe, w, _B_NT), 0.0)
        dkk = da * decay
        dqk = dp * decay
        dd = da * kk + dp * qk
        dq = dq + _bdot(dqk, kv)
        dk = dk + _bdot(dqk, qv, _B_TN)
        dkb = _bdot(dkk, kv) + dkbe * e
        dk = dk + _bdot(dkk, kb, _B_TN)
        de = de + rowsum(dkbe * kb)
        dk = dk + dkb * beta
        dbeta = rowsum(dkb * kv) + rowsum(dvb * vv)
        m = dd * decay
        dgc = dgc + rowsum(m) - rowsum(jnp.swapaxes(m, 1, 2))
        dgc = dgc + de * e[:, :, 0:1]
        dgc = dgc + jnp.where(row[:, 0:1] == c - 1, dglast, 0.0)
        upper = jnp.broadcast_to((row <= col).astype(F32), (nh, c, c))
        dg = _bdot(upper, jnp.broadcast_to(dgc, (nh, c, c)), prec=HIGHEST)
        dv = dvb * beta
        for h in range(nh):
            cols = slice(h * HEAD_A, (h + 1) * HEAD_A)
            dq_ref[:, cols] = dq[h]
            dk_ref[:, cols] = dk[h]
            dv_ref[:, cols] = dv[h]
        head = lax.broadcasted_iota(jnp.int32, (nh, c, LANE), 0)
        lane = lax.broadcasted_iota(jnp.int32, (nh, c, LANE), 2)
        dgb_ref[...] = jnp.where(lane == head, dbeta, jnp.where(lane == head + nh, dg, 0.0))

    blk = pl.BlockSpec((c, WIDTH_A), lambda i: (n - 1 - i, 0))
    mat = pl.BlockSpec((nh, 1, c, c), lambda i: (0, n - 1 - i, 0, 0))
    return _call(
        body, name=name, grid=(n,),
        in_specs=[blk, blk, blk, pl.BlockSpec((c, LANE), lambda i: (n - 1 - i, 0)), mat, mat, blk],
        out_specs=[blk, blk, blk, pl.BlockSpec((nh, c, LANE), lambda i: (0, n - 1 - i, 0))],
        out_shape=[jax.ShapeDtypeStruct((t, WIDTH_A), F32)] * 3 + [jax.ShapeDtypeStruct((nh, t, LANE), F32)],
        scratch_shapes=[pltpu.VMEM((nh, HEAD_A, HEAD_A), F32)], sem=("arbitrary",),
        args=(q, k, v, gates, tm_all, s_all, do), comm=comm)


def _onorm_fwd(o, gate, g, name):
    t = o.shape[0]

    def body(o_ref, gate_ref, g_ref, y_ref):
        ov, gv = o_ref[...], gate_ref[...]
        r = lax.rsqrt(jnp.mean(ov * ov, axis=-1, keepdims=True) + RMS_EPS)
        y_ref[...] = (ov * r * g_ref[...] * gv * _sigmoid(gv)).astype(BF16)

    blk = pl.BlockSpec((t, HEAD_A), lambda j: (0, j))
    return pl.pallas_call(
        body, name=name, grid=(N_HEADS_A,), in_specs=[blk, blk, pl.BlockSpec((1, HEAD_A), lambda j: (0, 0))],
        out_specs=blk, out_shape=jax.ShapeDtypeStruct((t, WIDTH_A), BF16),
        compiler_params=_params(("parallel",)))(o, gate, g)


def _onorm_bwd(o, gate, g, dy, name):
    t = o.shape[0]

    def body(o_ref, gate_ref, g_ref, dy_ref, do_ref, dgate_ref, dg_ref):
        @pl.when(pl.program_id(0) == 0)
        def _():
            dg_ref[...] = jnp.zeros_like(dg_ref)

        ov, gv, dyv = o_ref[...], gate_ref[...], dy_ref[...].astype(F32)
        r = lax.rsqrt(jnp.mean(ov * ov, axis=-1, keepdims=True) + RMS_EPS)
        oh = ov * r
        sg, dsg = _silu_and_grad(gv)
        dgate_ref[...] = (dyv * oh * g_ref[...] * dsg).astype(BF16)
        dn = dyv * sg
        dg_ref[...] += jnp.sum(dn * oh, axis=0, keepdims=True)
        dng = dn * g_ref[...]
        do_ref[...] = r * (dng - oh * jnp.mean(dng * oh, axis=-1, keepdims=True))

    blk = pl.BlockSpec((t, HEAD_A), lambda j: (0, j))
    vec = pl.BlockSpec((1, HEAD_A), lambda j: (0, 0))
    return pl.pallas_call(
        body, name=name, grid=(N_HEADS_A,), in_specs=[blk, blk, vec, blk], out_specs=[blk, blk, vec],
        out_shape=[jax.ShapeDtypeStruct((t, WIDTH_A), F32), jax.ShapeDtypeStruct((t, WIDTH_A), BF16),
                   jax.ShapeDtypeStruct((1, HEAD_A), F32)],
        compiler_params=_params(("arbitrary",)))(o, gate, g, dy)


def _cmul(ar, ai, br, bi):
    return ar * br - ai * bi, ar * bi + ai * br


def _scan_tables(ar, ai, reverse):
    p1 = (ar, ai)
    p2 = _cmul(*p1, *p1)
    p4 = _cmul(*p2, *p2)
    p8 = _cmul(*p4, *p4)
    p3 = _cmul(*p2, *p1)
    p5 = _cmul(*p4, *p1)
    p6 = _cmul(*p4, *p2)
    p7 = _cmul(*p4, *p3)
    pows = [p1, p2, p3, p4, p5, p6, p7, p8]
    rows = lax.broadcasted_iota(jnp.int32, (8, ar.shape[1]), 0)
    tr = jnp.zeros((8, ar.shape[1]), F32)
    ti = jnp.zeros((8, ar.shape[1]), F32)
    for r in range(8):
        pw = pows[7 - r] if reverse else pows[r]
        tr = jnp.where(rows == r, pw[0], tr)
        ti = jnp.where(rows == r, pw[1], ti)
    return p1, p2, p4, p8, tr, ti


def _tile_scan(xr, xi, p1, p2, p4, reverse):
    rows = lax.broadcasted_iota(jnp.int32, xr.shape, 0)
    for s, (pr, pi) in ((1, p1), (2, p2), (4, p4)):
        if reverse:
            keep = rows < 8 - s
            sr, si = pltpu.roll(xr, 8 - s, 0), pltpu.roll(xi, 8 - s, 0)
        else:
            keep = rows >= s
            sr, si = pltpu.roll(xr, s, 0), pltpu.roll(xi, s, 0)
        sr, si = jnp.where(keep, sr, 0.0), jnp.where(keep, si, 0.0)
        mr, mi = _cmul(pr, pi, sr, si)
        xr, xi = xr + mr, xi + mi
    return xr, xi


def _s5_scan_fwd(bu, a, name, tb=512, comm=None):
    t = bu.shape[0]
    cb = SCAN_CB
    nt = t // tb

    def body(b_ref, a_ref, x_ref, carry):
        @pl.when(pl.program_id(1) == 0)
        def _():
            carry[...] = jnp.zeros_like(carry)

        ar, ai = a_ref[:, 0:cb], a_ref[:, cb:2 * cb]
        p1, p2, p4, p8, tr, ti = _scan_tables(ar, ai, False)

        def step(j, c):
            cr, ci = c
            i = pl.multiple_of(j * 8, 8)
            xr, xi = _tile_scan(b_ref[pl.ds(i, 8), 0:cb], b_ref[pl.ds(i, 8), cb:2 * cb], p1, p2, p4, False)
            mr, mi = _cmul(tr, ti, cr, ci)
            xr, xi = xr + mr, xi + mi
            x_ref[pl.ds(i, 8), 0:cb] = xr
            x_ref[pl.ds(i, 8), cb:2 * cb] = xi
            return xr[7:8, :], xi[7:8, :]

        cr, ci = lax.fori_loop(0, tb // 8, step, (carry[0:1, :], carry[1:2, :]), unroll=2)
        carry[0:1, :] = cr
        carry[1:2, :] = ci

    blk = pl.BlockSpec((tb, 2 * cb), lambda j, i: (i, j))
    return _call(
        body, name=name, grid=(SSM_CH // cb, nt),
        in_specs=[blk, pl.BlockSpec((1, 2 * cb), lambda j, i: (0, j))], out_specs=blk,
        out_shape=jax.ShapeDtypeStruct((t, 2 * SSM_CH), F32), scratch_shapes=[pltpu.VMEM((8, cb), F32)],
        sem=("parallel", "arbitrary"), args=(bu, a), comm=comm)


def _s5_scan_bwd(dx, x, a, name, tb=512, comm=None):
    t = dx.shape[0]
    cb = SCAN_CB
    nt = t // tb
    nj = tb // 8

    def body(d_ref, x_ref, xp_ref, a_ref, l_ref, da_ref, carry, acc):
        tblk = pl.program_id(1)

        @pl.when(tblk == 0)
        def _():
            carry[...] = jnp.zeros_like(carry)
            acc[...] = jnp.zeros_like(acc)

        ar, ai = a_ref[:, 0:cb], a_ref[:, cb:2 * cb]
        p1, p2, p4, p8, tr, ti = _scan_tables(ar, -ai, True)
        rows = lax.broadcasted_iota(jnp.int32, (8, cb), 0)

        def step(jj, c):
            cr, ci, sr_acc, si_acc = c
            j = nj - 1 - jj
            i = pl.multiple_of(j * 8, 8)
            lr, li = _tile_scan(d_ref[pl.ds(i, 8), 0:cb], d_ref[pl.ds(i, 8), cb:2 * cb], p1, p2, p4, True)
            mr, mi = _cmul(tr, ti, cr, ci)
            lr, li = lr + mr, li + mi
            l_ref[pl.ds(i, 8), 0:cb] = lr
            l_ref[pl.ds(i, 8), cb:2 * cb] = li
            ip = pl.multiple_of(jnp.maximum(j - 1, 0) * 8, 8)
            prev_r = jnp.where(j > 0, x_ref[pl.ds(ip, 8), 0:cb], xp_ref[:, 0:cb])
            prev_i = jnp.where(j > 0, x_ref[pl.ds(ip, 8), cb:2 * cb], xp_ref[:, cb:2 * cb])
            edge = jnp.where(jnp.logical_and(j == 0, tblk == nt - 1), 0.0, 1.0)
            xs_r = jnp.where(rows == 0, pltpu.roll(prev_r, 1, 0) * edge, pltpu.roll(x_ref[pl.ds(i, 8), 0:cb], 1, 0))
            xs_i = jnp.where(rows == 0, pltpu.roll(prev_i, 1, 0) * edge, pltpu.roll(x_ref[pl.ds(i, 8), cb:2 * cb], 1, 0))
            sr_acc = sr_acc + lr * xs_r + li * xs_i
            si_acc = si_acc + li * xs_r - lr * xs_i
            return lr[0:1, :], li[0:1, :], sr_acc, si_acc

        cr, ci, sr_acc, si_acc = lax.fori_loop(
            0, nj, step, (carry[0:1, :], carry[1:2, :], acc[:, 0:cb], acc[:, cb:2 * cb]))
        carry[0:1, :] = cr
        carry[1:2, :] = ci
        acc[:, 0:cb] = sr_acc
        acc[:, cb:2 * cb] = si_acc

        @pl.when(tblk == nt - 1)
        def _():
            da_ref[...] = jnp.sum(acc[...], axis=0, keepdims=True)

    blk = pl.BlockSpec((tb, 2 * cb), lambda j, i: (nt - 1 - i, j))
    prev = pl.BlockSpec((8, 2 * cb), lambda j, i: (jnp.maximum((nt - 1 - i) * (tb // 8) - 1, 0), j))
    vec = pl.BlockSpec((1, 2 * cb), lambda j, i: (0, j))
    return _call(
        body, name=name, grid=(SSM_CH // cb, nt), in_specs=[blk, blk, prev, vec], out_specs=[blk, vec],
        out_shape=[jax.ShapeDtypeStruct((t, 2 * SSM_CH), F32), jax.ShapeDtypeStruct((1, 2 * SSM_CH), F32)],
        scratch_shapes=[pltpu.VMEM((8, cb), F32), pltpu.VMEM((8, 2 * cb), F32)],
        sem=("parallel", "arbitrary"), args=(dx, x, x, a), comm=comm)


def _glu_fwd(yc, u, dvec, wg, bg, name, tr=256):
    t = yc.shape[0]

    def body(yc_ref, u_ref, d_ref, w_ref, b_ref, yl_ref, yb_ref):
        yl = yc_ref[...] + d_ref[...] * u_ref[...]
        yl_ref[...] = yl
        yg, _ = _gelu_and_grad(yl)
        z = jnp.dot(yg.astype(BF16), w_ref[...], preferred_element_type=F32) + b_ref[...]
        yb_ref[...] = (yg * _sigmoid(z)).astype(BF16)

    blk = pl.BlockSpec((tr, SSM_WIDTH), lambda i: (i, 0))
    vec = pl.BlockSpec((1, SSM_WIDTH), lambda i: (0, 0))
    return pl.pallas_call(
        body, name=name, grid=(t // tr,),
        in_specs=[blk, blk, vec, pl.BlockSpec((SSM_WIDTH, SSM_WIDTH), lambda i: (0, 0)), vec],
        out_specs=[blk, blk],
        out_shape=[jax.ShapeDtypeStruct((t, SSM_WIDTH), F32), jax.ShapeDtypeStruct((t, SSM_WIDTH), BF16)],
        compiler_params=_params(("parallel",)))(yc, u, dvec, wg, bg)


def _glu_bwd(yl, u, dvec, wg, bg, dyb, name, tr=256):
    t = yl.shape[0]

    def body(yl_ref, u_ref, d_ref, w_ref, b_ref, dy_ref, dyl_ref, du_ref, dw_ref, db_ref, dd_ref):
        @pl.when(pl.program_id(0) == 0)
        def _():
            dw_ref[...] = jnp.zeros_like(dw_ref)
            db_ref[...] = jnp.zeros_like(db_ref)
            dd_ref[...] = jnp.zeros_like(dd_ref)

        ylv, dyv, wv = yl_ref[...], dy_ref[...].astype(F32), w_ref[...]
        yg, dgelu = _gelu_and_grad(ylv)
        ygb = yg.astype(BF16)
        z = jnp.dot(ygb, wv, preferred_element_type=F32) + b_ref[...]
        sg = _sigmoid(z)
        dz = dyv * yg * sg * (1.0 - sg)
        dzb = dz.astype(BF16)
        dyg = dyv * sg + lax.dot_general(dzb, wv, (((1,), (1,)), ((), ())), preferred_element_type=F32)
        dyl = dyg * dgelu
        dyl_ref[...] = dyl.astype(BF16)
        du_ref[...] = dyl * d_ref[...]
        dw_ref[...] += lax.dot_general(ygb, dzb, (((0,), (0,)), ((), ())), preferred_element_type=F32)
        db_ref[...] += jnp.sum(dz, axis=0, keepdims=True)
        dd_ref[...] += jnp.sum(dyl * u_ref[...], axis=0, keepdims=True)

    blk = pl.BlockSpec((tr, SSM_WIDTH), lambda i: (i, 0))
    vec = pl.BlockSpec((1, SSM_WIDTH), lambda i: (0, 0))
    wsp = pl.BlockSpec((SSM_WIDTH, SSM_WIDTH), lambda i: (0, 0))
    return pl.pallas_call(
        body, name=name, grid=(t // tr,), in_specs=[blk, blk, vec, wsp, vec, blk],
        out_specs=[blk, blk, wsp, vec, vec],
        out_shape=[jax.ShapeDtypeStruct((t, SSM_WIDTH), BF16), jax.ShapeDtypeStruct((t, SSM_WIDTH), F32),
                   jax.ShapeDtypeStruct((SSM_WIDTH, SSM_WIDTH), F32), jax.ShapeDtypeStruct((1, SSM_WIDTH), F32),
                   jax.ShapeDtypeStruct((1, SSM_WIDTH), F32)],
        compiler_params=_params(("arbitrary",)))(yl, u, dvec, wg, bg, dyb)


def _mesh_pos():
    return lax.axis_index("x"), lax.axis_index("y"), lax.axis_index("c")


def _device_index():
    x, y, c = _mesh_pos()
    return 4 * x + 2 * y + c


def _gather_comm(arrays):
    na = len(arrays)

    def own_copy(ins, outs, sems, ai):
        return pltpu.make_async_copy(ins[ai], outs[ai].at[_device_index()], sems[2].at[ai])

    def ctx(ins, outs, sems):
        send_sems, recv_sems = sems[:2]
        x, y, c = _mesh_pos()
        chips = [(1 - x, y), (x, 1 - y), (1 - x, 1 - y)]

        def copy(ai, kk, block, to, own=False):
            slot = outs[ai].at[4 * block[0] + 2 * block[1] + block[2]]
            return pltpu.make_async_remote_copy(
                src_ref=ins[ai] if own else slot, dst_ref=slot, send_sem=send_sems.at[ai, kk],
                recv_sem=recv_sems.at[ai, kk], device_id=to, device_id_type=MESH)

        return (x, y, c), (x, y, 1 - c), chips, c, copy

    def start(ins, outs, sems):
        me, sibling, chips, c, copy = ctx(ins, outs, sems)
        for ai in range(na):
            copy(ai, 0, me, sibling, own=True).start()
            for j, chip in enumerate(chips):
                copy(ai, 1 + j, me, (*chip, c), own=True).start()
        for ai in range(na):
            own_copy(ins, outs, sems, ai).start()

    def mid(ins, outs, sems):
        me, sibling, chips, c, copy = ctx(ins, outs, sems)
        for ai in range(na):
            for j, chip in enumerate(chips):
                copy(ai, 1 + j, (*chip, c), me).wait_recv()
                copy(ai, 4 + j, (*chip, c), sibling).start()

    def end(ins, outs, sems):
        me, sibling, chips, c, copy = ctx(ins, outs, sems)
        for ai in range(na):
            copy(ai, 0, sibling, me).wait_recv()
            copy(ai, 0, me, sibling, own=True).wait_send()
            for j, chip in enumerate(chips):
                copy(ai, 4 + j, (*chip, 1 - c), me).wait_recv()
                copy(ai, 1 + j, me, (*chip, c), own=True).wait_send()
                copy(ai, 4 + j, (*chip, c), sibling).wait_send()
            own_copy(ins, outs, sems, ai).wait()

    return Comm(arrays, [jax.ShapeDtypeStruct((N_DEV,) + a.shape, a.dtype) for a in arrays],
                [pltpu.SemaphoreType.DMA((na, 7)), pltpu.SemaphoreType.DMA((na, 7)), pltpu.SemaphoreType.DMA((na,))],
                start, end, mid)


def _sequencer_gather(arrays, name, collective_id):
    comm = _gather_comm(arrays)
    na = len(arrays)

    def body(*refs):
        ins, outs, sems = refs[:na], refs[na:2 * na], refs[2 * na:]
        x, y, c = _mesh_pos()
        peers = [(x, y, 1 - c), (1 - x, y, c), (x, 1 - y, c), (1 - x, 1 - y, c)]
        barrier = pltpu.get_barrier_semaphore()
        for peer in peers:
            pl.semaphore_signal(barrier, inc=1, device_id=peer, device_id_type=MESH)
        pl.semaphore_wait(barrier, len(peers))
        comm.start(ins, outs, sems)
        comm.mid(ins, outs, sems)
        comm.end(ins, outs, sems)

    return list(pl.kernel(
        body, out_type=tuple(comm.out_shapes), mesh=plsc.ScalarSubcoreMesh(axis_name="sequencer", num_cores=1),
        name=name, scratch_types=tuple(comm.sems),
        compiler_params=pltpu.CompilerParams(collective_id=collective_id))(*arrays))


def _sequencer_exchange(comm, peers_of, name, collective_id):
    na = len(comm.inputs)

    def body(*refs):
        ins, outs, sems = refs[:na], refs[na:na + len(comm.out_shapes)], refs[na + len(comm.out_shapes):]
        peers = peers_of(*_mesh_pos())
        barrier = pltpu.get_barrier_semaphore()
        for peer in peers:
            pl.semaphore_signal(barrier, inc=1, device_id=peer, device_id_type=MESH)
        pl.semaphore_wait(barrier, len(peers))
        comm.start(ins, outs, sems)
        comm.end(ins, outs, sems)

    return list(pl.kernel(
        body, out_type=tuple(comm.out_shapes), mesh=plsc.ScalarSubcoreMesh(axis_name="sequencer", num_cores=1),
        name=name, scratch_types=tuple(comm.sems),
        compiler_params=pltpu.CompilerParams(collective_id=collective_id))(*comm.inputs))


SIBLING_SWAP_ID, CHIP_EXCHANGE_ID = 9, 10


def _sequencer_swap(arrays, name):
    return _sequencer_exchange(_swap_comm(arrays), lambda x, y, c: [(x, y, 1 - c)], name, SIBLING_SWAP_ID)[0]


def _sequencer_chips(send, name):
    return _sequencer_exchange(_chips_comm(send), lambda x, y, c: [(1 - x, y, c), (x, 1 - y, c), (1 - x, 1 - y, c)],
                               name, CHIP_EXCHANGE_ID)[0]


def _swap_comm(arrays):
    na = len(arrays)
    offs = np.concatenate([[0], np.cumsum([a.shape[1] for a in arrays])]).astype(int)

    def copies(ins, outs, sems):
        x, y, c = _mesh_pos()
        return [pltpu.make_async_remote_copy(
            src_ref=ins[ai].at[2 * k + 1 - c], dst_ref=outs[0].at[k, pl.ds(int(offs[ai]), arrays[ai].shape[1])],
            send_sem=sems[0].at[ai, k], recv_sem=sems[1].at[ai, k], device_id=(x, y, 1 - c), device_id_type=MESH)
            for ai in range(na) for k in range(4)]

    def start(ins, outs, sems):
        for cp in copies(ins, outs, sems):
            cp.start()

    def end(ins, outs, sems):
        for cp in copies(ins, outs, sems):
            cp.wait()

    return Comm(arrays, [jax.ShapeDtypeStruct((4, int(offs[-1]), PACK_COLS), arrays[0].dtype)],
                [pltpu.SemaphoreType.DMA((na, 4)), pltpu.SemaphoreType.DMA((na, 4))], start, end)


def _chips_comm(send):
    def copies(ins, outs, sems):
        x, y, c = _mesh_pos()
        chips = [(1 - x, y), (x, 1 - y), (1 - x, 1 - y)]
        return [pltpu.make_async_remote_copy(
            src_ref=ins[0].at[2 * cx + cy], dst_ref=outs[0].at[j], send_sem=sems[0].at[j], recv_sem=sems[1].at[j],
            device_id=(cx, cy, c), device_id_type=MESH) for j, (cx, cy) in enumerate(chips)]

    def start(ins, outs, sems):
        for cp in copies(ins, outs, sems):
            cp.start()

    def end(ins, outs, sems):
        for cp in copies(ins, outs, sems):
            cp.wait()

    return Comm([send], [jax.ShapeDtypeStruct((3,) + send.shape[1:], send.dtype)],
                [pltpu.SemaphoreType.DMA((3,)), pltpu.SemaphoreType.DMA((3,))], start, end)


def _pair_sum(keep, recv, name, tr=464):
    nchip, rows, cols = keep.shape

    def body(g_ref, r_ref, o_ref):
        o_ref[...] = (g_ref[...].astype(F32) + r_ref[...].astype(F32)).astype(BF16)

    blk = pl.BlockSpec((1, tr, cols), lambda k, i: (k, i, 0))
    return pl.pallas_call(
        body, name=name, grid=(nchip, rows // tr), in_specs=[blk, blk], out_specs=blk,
        out_shape=jax.ShapeDtypeStruct((nchip, rows, cols), BF16),
        compiler_params=_params(("parallel", "parallel")))(keep, recv)


def _pair_sum_pieces(pieces, recv, name, tr):
    _, rows, cols = pieces.shape
    core = lax.axis_index("c").astype(jnp.int32).reshape(1)

    def body(c_ref, g_ref, r_ref, o_ref):
        del c_ref
        o_ref[...] = (g_ref[...].astype(F32) + r_ref[...].astype(F32)).astype(BF16)

    grid_spec = pltpu.PrefetchScalarGridSpec(
        num_scalar_prefetch=1, grid=(4, rows // tr),
        in_specs=[pl.BlockSpec((1, tr, cols), lambda k, i, c_ref: (2 * k + c_ref[0], i, 0)),
                  pl.BlockSpec((1, tr, cols), lambda k, i, c_ref: (k, i, 0))],
        out_specs=pl.BlockSpec((1, tr, cols), lambda k, i, c_ref: (k, i, 0)))
    return pl.pallas_call(
        body, name=name, grid_spec=grid_spec, out_shape=jax.ShapeDtypeStruct((4, rows, cols), BF16),
        compiler_params=_params(("parallel", "parallel")))(core, pieces, recv)


def _chip_sum(own, others, name, tr=464):
    _, rows, cols = own.shape
    chip = (2 * lax.axis_index("x") + lax.axis_index("y")).astype(jnp.int32).reshape(1)

    def body(chip_ref, own_ref, oth_ref, o_ref):
        del chip_ref
        acc = own_ref[0].astype(F32)
        for j in range(3):
            acc = acc + oth_ref[j].astype(F32)
        o_ref[...] = acc

    grid_spec = pltpu.PrefetchScalarGridSpec(
        num_scalar_prefetch=1, grid=(rows // tr,),
        in_specs=[pl.BlockSpec((1, tr, cols), lambda i, chip_ref: (chip_ref[0], i, 0)),
                  pl.BlockSpec((3, tr, cols), lambda i, chip_ref: (0, i, 0))],
        out_specs=pl.BlockSpec((tr, cols), lambda i, chip_ref: (i, 0)))
    return pl.pallas_call(
        body, name=name, grid_spec=grid_spec, out_shape=jax.ShapeDtypeStruct((rows, cols), F32),
        compiler_params=_params(("parallel",)))(chip, own, others)


def _sum_leading(parts, name, tr=464):
    nparts, rows, cols = parts.shape
    tr = tr if rows % tr == 0 else rows

    def body(p_ref, o_ref):
        acc = p_ref[0].astype(F32)
        for i in range(1, nparts):
            acc = acc + p_ref[i].astype(F32)
        o_ref[...] = acc

    return pl.pallas_call(
        body, name=name, grid=(rows // tr,),
        in_specs=[pl.BlockSpec((nparts, tr, cols), lambda i: (0, i, 0))],
        out_specs=pl.BlockSpec((tr, cols), lambda i: (i, 0)), out_shape=jax.ShapeDtypeStruct((rows, cols), F32),
        compiler_params=_params(("parallel",)))(parts)


def _adamw(w, g, m, v, name, comm=None):
    shape = w.shape
    cols = shape[-1]
    lead = shape[0] if len(shape) >= 3 else 1
    rows = int(np.prod(shape[:-1])) // lead if len(shape) > 1 else 1
    w2, g2, m2, v2 = (a.reshape(lead, rows, cols) for a in (w, g, m, v))
    tr = rows
    for cand in (512, 256, 128, 64, 32, 16, 8):
        if rows % cand == 0 and rows > cand:
            tr = cand
            break
    bc1, bc2 = 1.0 - ADAM_B1 ** ADAM_STEP, 1.0 - ADAM_B2 ** ADAM_STEP

    def body(w_ref, g_ref, m_ref, v_ref, d_ref, nm_ref, nv_ref):
        gv = g_ref[...]
        nm = ADAM_B1 * m_ref[...] + (1.0 - ADAM_B1) * gv
        nv = ADAM_B2 * v_ref[...] + (1.0 - ADAM_B2) * (gv * gv)
        nm_ref[...] = nm
        nv_ref[...] = nv
        d_ref[...] = -ADAM_LR * ((nm / bc1) / (jnp.sqrt(nv / bc2) + ADAM_EPS) + ADAM_WD * w_ref[...])

    blk = pl.BlockSpec((1, tr, cols), lambda l, i: (l, i, 0))
    res = _call(body, name=name, grid=(lead, rows // tr), in_specs=[blk] * 4, out_specs=[blk] * 3,
                out_shape=[jax.ShapeDtypeStruct((lead, rows, cols), F32)] * 3, sem=("parallel", "parallel"),
                args=(w2, g2, m2, v2), comm=comm)
    outs, couts = res if comm is not None else (res, None)
    outs = tuple(o.reshape(shape) for o in outs)
    return outs if comm is None else (outs, couts)


WEIGHT_NAMES = ['norm_mix_g', 'norm_xa_g', 'norm_ffn_g', 'norm_mem_g', 'norm_final_g', 'w_in_ab', 'conv_qkv_a',
                'a_log_a', 'dt_bias_a', 'onorm_g_a', 'ssm_lambda_re', 'ssm_lambda_im', 'ssm_b_re', 'ssm_b_im',
                'ssm_c_re', 'ssm_c_im', 'ssm_d', 'ssm_log_dt', 'w_glu_b', 'b_glu_b', 'w_out_ab', 'pool_w',
                'pool_scale', 'xa_wq', 'xa_wkv', 'xa_wo', 'ffn_w_up', 'ffn_conv', 'ffn_w_down']
BIG_SHARDED = {'w_in_ab': ((1, 1024, 2568), 2), 'w_glu_b': ((1, 512, 512), 1), 'w_out_ab': ((1, 1024, 1024), 1),
               'pool_w': ((1, 4, 256, 256), 2), 'xa_wq': ((2, 1024, 1024), 1), 'xa_wkv': ((2, 1024, 2048), 2),
               'xa_wo': ((2, 1024, 1024), 1), 'ffn_w_up': ((2, 1024, 5632), 2), 'ffn_w_down': ((2, 2816, 1024), 1)}
SMALL_SHARDED = {'conv_qkv_a': ((1, 4, 1536), 2), 'pool_scale': ((1, 1024), 1), 'ffn_conv': ((2, 3, 5632), 2)}
REPLICATED = {'norm_mix_g': (2, 1024), 'norm_xa_g': (2, 1024), 'norm_ffn_g': (2, 1024), 'norm_mem_g': (1024,),
              'norm_final_g': (1024,), 'a_log_a': (1, 4), 'dt_bias_a': (1, 4), 'onorm_g_a': (1, 128),
              'ssm_lambda_re': (1, 32, 64), 'ssm_lambda_im': (1, 32, 64), 'ssm_b_re': (1, 32, 64, 16),
              'ssm_b_im': (1, 32, 64, 16), 'ssm_c_re': (1, 32, 16, 64), 'ssm_c_im': (1, 32, 16, 64),
              'ssm_d': (1, 32, 16), 'ssm_log_dt': (1, 32), 'b_glu_b': (1, 512)}
PACK_ROW_ALIGN = 8


def _shard_shape(shape, axis):
    return tuple(s // N_DEV if i == axis else s for i, s in enumerate(shape))


def _round_up(n, m):
    return (n + m - 1) // m * m


def _pack(arrays):
    total = sum(int(np.prod(a.shape)) for a in arrays)
    padded = _round_up(total, PACK_COLS * PACK_ROW_ALIGN)
    parts = [a.astype(F32).reshape(-1) for a in arrays]
    if padded != total:
        parts.append(jnp.zeros((padded - total,), F32))
    return jnp.concatenate(parts).reshape(padded // PACK_COLS, PACK_COLS)


def _unpack(packed, shapes):
    flat, out, off = packed.reshape(-1), [], 0
    for shape in shapes:
        size = int(np.prod(shape))
        out.append(flat[off:off + size].reshape(shape))
        off += size
    return out


def _split_shards(full, axis):
    shape = full.shape
    s = shape[axis] // N_DEV
    a = full.reshape(shape[:axis] + (N_DEV, s) + shape[axis + 1:])
    return jnp.moveaxis(a, axis, 0).reshape(N_DEV, -1)


def _merge_shards(pieces, shape, axis):
    sh = _shard_shape(shape, axis)
    a = pieces.reshape((N_DEV,) + sh)
    a = jnp.moveaxis(a, 0, axis)
    return a.reshape(shape)


_SCAN_NB = SSM_CH // SCAN_CB


def _to_scan_layout(m, axis):
    shape = m.shape
    m = m.reshape(shape[:axis] + (2, _SCAN_NB, SCAN_CB) + shape[axis + 1:])
    return jnp.swapaxes(m, axis, axis + 1).reshape(shape)


def _from_scan_layout(m, axis):
    shape = m.shape
    m = m.reshape(shape[:axis] + (_SCAN_NB, 2, SCAN_CB) + shape[axis + 1:])
    return jnp.swapaxes(m, axis, axis + 1).reshape(shape)


def _s5_discretise(lam_re, lam_im, b_re, b_im, log_dt):
    dt = jnp.exp(log_dt)[:, None]
    mag = jnp.exp(lam_re * dt)
    ang = lam_im * dt
    lb_re, lb_im = mag * jnp.cos(ang), mag * jnp.sin(ang)
    den = lam_re * lam_re + lam_im * lam_im
    nr, ni = lb_re - 1.0, lb_im
    coef_re = (nr * lam_re + ni * lam_im) / den
    coef_im = (ni * lam_re - nr * lam_im) / den
    bb_re = coef_re[..., None] * b_re - coef_im[..., None] * b_im
    bb_im = coef_re[..., None] * b_im + coef_im[..., None] * b_re
    return lb_re, lb_im, bb_re, bb_im


_GROUPS_PER_BLOCK = N_GROUPS // _SCAN_NB
_U_BLOCK = _GROUPS_PER_BLOCK * SSM_GROUP


def _s5_matrices(lb_re, lb_im, bb_re, bb_im, c_re, c_im):
    eye = jnp.eye(_GROUPS_PER_BLOCK, dtype=F32)
    blocked = lambda m: m.reshape((_SCAN_NB, _GROUPS_PER_BLOCK) + m.shape[1:])
    bmat = lambda bb: jnp.einsum('jgph,gk->jghkp', blocked(bb), eye).reshape(_SCAN_NB, _U_BLOCK, SCAN_CB)
    cmat = lambda cc: jnp.einsum('jghp,gk->jkpgh', blocked(cc), eye).reshape(_SCAN_NB, SCAN_CB, _U_BLOCK)
    b_in = jnp.concatenate([bmat(bb_re), bmat(bb_im)], axis=2)
    c_out = jnp.concatenate([cmat(c_re), -cmat(c_im)], axis=1)
    a_row = _to_scan_layout(jnp.concatenate([lb_re.reshape(1, SSM_CH), lb_im.reshape(1, SSM_CH)], axis=1), 1)
    return b_in, c_out, a_row


def _s5_matrix_grads(db_in, dc_out, da_row):
    da_nat = _from_scan_layout(da_row, 1)
    eye = jnp.eye(_GROUPS_PER_BLOCK, dtype=F32)
    nb, gb = _SCAN_NB, _GROUPS_PER_BLOCK
    bgrad = lambda m: jnp.einsum('jghkp,gk->jgph', m.reshape(nb, gb, SSM_GROUP, gb, SSM_STATE), eye
                                 ).reshape(N_GROUPS, SSM_STATE, SSM_GROUP)
    cgrad = lambda m: jnp.einsum('jkpgh,gk->jghp', m.reshape(nb, gb, SSM_STATE, gb, SSM_GROUP), eye
                                 ).reshape(N_GROUPS, SSM_GROUP, SSM_STATE)
    dbb_re, dbb_im = bgrad(db_in[:, :, :SCAN_CB]), bgrad(db_in[:, :, SCAN_CB:])
    dc_re, dc_im = cgrad(dc_out[:, :SCAN_CB]), -cgrad(dc_out[:, SCAN_CB:])
    dlb_re = da_nat[0, :SSM_CH].reshape(N_GROUPS, SSM_STATE)
    dlb_im = da_nat[0, SSM_CH:].reshape(N_GROUPS, SSM_STATE)
    return dlb_re, dlb_im, dbb_re, dbb_im, dc_re, dc_im


def _as_pieces(a):
    return a.reshape(N_DEV, a.shape[0] // N_DEV, a.shape[1])


def _hybrid_fwd(xn, x, wts, p, weights, riders):
    sv = {}
    hq = _mm(xn, wts['w_qkv_t'], "nt", "l0_in_qkv")
    gate = _mm(xn, wts['w_gate_t'], "nt", "l0_in_gate")
    ba = _mm(xn, wts['w_ba_t'], "nt", "l0_in_ba")
    u = _mm(xn, wts['w_u_t'], "nt", "l0_in_u")
    conv = p['conv_qkv']
    q = _qkv_pre_fwd(hq, conv, 0, 4, True, HEAD_A ** -0.5, "l0_q_pre")
    k = _qkv_pre_fwd(hq, conv, 4, 4, True, 1.0, "l0_k_pre")
    v = _qkv_pre_fwd(hq, conv, 8, 4, False, 1.0, "l0_v_pre")
    gates = _gates_fwd(ba, p['arow'], p['brow'], "l0_gates")
    o, tm_all, s_all = riders.run("l0_gdr_fwd", _gdr_fwd, q, k, v, gates)
    wts['w_glu'], wts['w_out'] = weights.full['w_glu'], weights.full['w_out']
    y_a = _onorm_fwd(o, gate, p['onorm_g'], "l0_onorm")
    bu = riders.run("l0_s5_bu", _mm_bd, u, p['b_in'], "nn")
    xs = riders.run("l0_s5_scan", _s5_scan_fwd, bu, p['a_row'])
    weights.gather_by_sequencer(GATHER_LAYER1, xs, "gather_layer1", GATHER_LAYER1_ID)
    yc = riders.run("l0_s5_cx", _mm_bd, xs, p['c_out'], "nn")
    yl, y_b = _glu_fwd(yc, u, p['d_row'], wts['w_glu'], p['b_glu'], "l0_glu")
    mixed = jnp.concatenate([y_a, y_b], axis=1)
    x1 = _mm(mixed, wts['w_out'], "nn", "l0_out", res=x)
    sv.update(hq=hq, gate=gate, ba=ba, u=u, q=q, k=k, v=v, gb=gates, o=o, tm=tm_all, s=s_all, xs=xs, yl=yl, mixed=mixed)
    return x1, sv


def _hybrid_bwd(dx1, xn, wts, p, sv, riders):
    gr = {}
    dmixed = _mm(dx1, wts['w_out'], "nt", "l0_out_dx", out_dtype=BF16)
    riders.grad('w_out', _as_pieces(_mm(sv['mixed'], dx1, "tn", "l0_out_dw", out_dtype=BF16)))
    dya, dyb = dmixed[:, :WIDTH_A], dmixed[:, WIDTH_A:]
    dyl, du_direct, dw_glu, gr['b_glu_b'], dd = _glu_bwd(
        sv['yl'], sv['u'], p['d_row'], wts['w_glu'], p['b_glu'], dyb, "l0_glu_bwd")
    riders.grad('w_glu', dw_glu.astype(BF16).reshape(N_DEV, -1, PACK_COLS))
    dxs = riders.run("l0_s5_cx_dx", _mm_bd, dyl, p['c_out'], "nt")
    dc_out = _mm_bd(sv['xs'], dyl, "tn", "l0_s5_cx_dw")
    lam, da_row = riders.run("l0_s5_scan_bwd", _s5_scan_bwd, dxs, sv['xs'], p['a_row'])
    du = _mm_bd(lam, p['b_in'], "nt", "l0_s5_bu_dx", res=du_direct, out_dtype=BF16)
    db_in = _mm_bd(sv['u'], lam, "tn", "l0_s5_bu_dw")
    gr['s5'] = (db_in, dc_out, da_row, dd)
    do, dgate, gr['onorm_g_a'] = _onorm_bwd(sv['o'], sv['gate'], p['onorm_g'], dya, "l0_onorm_bwd")
    dq, dk, dv, dgb = riders.run("l0_gdr_bwd", _gdr_bwd, sv['q'], sv['k'], sv['v'], sv['gb'], sv['tm'], sv['s'], do)
    conv = p['conv_qkv']
    dhq_q, dcw_q = _qkv_pre_bwd(sv['hq'], conv, dq, 0, 4, True, HEAD_A ** -0.5, "l0_q_pre_bwd")
    dhq_k, dcw_k = _qkv_pre_bwd(sv['hq'], conv, dk, 4, 4, True, 1.0, "l0_k_pre_bwd")
    dhq_v, dcw_v = _qkv_pre_bwd(sv['hq'], conv, dv, 8, 4, False, 1.0, "l0_v_pre_bwd")
    gr['conv_qkv_a'] = jnp.concatenate([dcw_q, dcw_k, dcw_v], axis=1)
    dhq = jnp.concatenate([dhq_q, dhq_k, dhq_v], axis=1)
    dba, da_log, ddt_bias = _gates_bwd(sv['ba'], p['arow'], p['brow'], dgb, "l0_gates_bwd")
    gr['a_log_a'], gr['dt_bias_a'] = da_log[:, 4:8], ddt_bias[:, 4:8]
    dw_qkv_t = _mm(dhq, xn, "tn", "l0_in_qkv_dw", out_dtype=BF16)
    dw_gate_t = _mm(dgate, xn, "tn", "l0_in_gate_dw", out_dtype=BF16)
    dw_ba_t = _mm(dba, xn, "tn", "l0_in_ba_dw", out_dtype=BF16)
    dw_u_t = _mm(du, xn, "tn", "l0_in_u_dw", out_dtype=BF16)
    dw_in_t = _as_pieces(jnp.concatenate([dw_qkv_t, dw_gate_t, dw_ba_t[:8], dw_u_t], axis=0))
    riders.grad('w_in_t', jnp.concatenate(
        [dw_in_t, jnp.zeros((N_DEV, dict(PIECES)['w_in_t'] - W_IN_PIECE, D_MODEL), BF16)], axis=1))
    dxn = riders.run("l0_in_qkv_dx", _mm, dhq, wts['w_qkv_t'], "nn")
    dxn = _mm(dgate, wts['w_gate_t'], "nn", "l0_in_gate_dx", res=dxn)
    dxn = _mm(dba, wts['w_ba_t'], "nn", "l0_in_ba_dx", res=dxn)
    dxn = riders.run("l0_in_u_dx", _mm, du, wts['w_u_t'], "nn", res=dxn)
    return dxn, gr


def _xa_fwd(x1, g, mem_n, wq, wkv_t, wo, tag, riders):
    xq = _rms_fwd(x1, g, BF16, tag + "_norm")
    q = _mm(xq, wq, "nn", tag + "_q", out_dtype=BF16)
    kv = _mm(mem_n, wkv_t, "nt", tag + "_kv", out_dtype=BF16)
    o = riders.run(tag + "_attn", _attn_fwd, q, kv)
    x2 = _mm(o, wo, "nn", tag + "_o", res=x1)
    return x2, dict(xq=xq, q=q, kv=kv, o=o)


def _xa_bwd(dx2, x1, g, mem_n, wq, wkv_t, wo, sv, tag, layer, riders):
    do = _mm(dx2, wo, "nt", tag + "_o_dx", out_dtype=BF16)
    riders.grad('wo%d' % layer, _as_pieces(_mm(sv['o'], dx2, "tn", tag + "_o_dw", out_dtype=BF16)))
    dq, dk, dv = _attn_bwd(sv['q'], sv['kv'], do, tag + "_attn_bwd")
    dkv = jnp.concatenate([dk, dv], axis=1).astype(BF16)
    dxq = _mm(dq, wq, "nt", tag + "_q_dx")
    riders.grad('wq%d' % layer, _as_pieces(_mm(sv['xq'], dq, "tn", tag + "_q_dw", out_dtype=BF16)))
    dmem_n = _mm(dkv, wkv_t, "nn", tag + "_kv_dx")
    riders.grad('wkv_t%d' % layer, _as_pieces(_mm(dkv, mem_n, "tn", tag + "_kv_dw", out_dtype=BF16)))
    dx1, dg = riders.run(tag + "_norm_bwd", _rms_bwd, x1, g, dxq, dx2)
    return dx1, dmem_n, dg


def _ffn_fwd(x2, g, w_up_t, conv, w_down, tag, riders):
    xf = _rms_fwd(x2, g, BF16, tag + "_norm")
    h = riders.run(tag + "_up", _mm, xf, w_up_t, "nt")
    a = riders.run(tag + "_act", _ffn_act_fwd, h, conv)
    x3 = _mm(a, w_down, "nn", tag + "_down", res=x2)
    return x3, dict(xf=xf, h=h, a=a)


def _ffn_bwd(dx3, x2, g, w_up_t, conv, w_down, sv, tag, layer, riders):
    da = _mm(dx3, w_down, "nt", tag + "_down_dx")
    riders.grad('down%d' % layer, _as_pieces(_mm(sv['a'], dx3, "tn", tag + "_down_dw", out_dtype=BF16)))
    dh, dconv = riders.run(tag + "_act_bwd", _ffn_act_bwd, sv['h'], conv, da)
    dxf = riders.run(tag + "_up_dx", _mm, dh, w_up_t, "nn")
    dw_up_t = riders.run(tag + "_up_dw", _mm, dh, sv['xf'], "tn", out_dtype=BF16)
    riders.grad('up_t%d' % layer, _as_pieces(dw_up_t))
    dx2, dg = riders.run(tag + "_norm_bwd", _rms_bwd, x2, g, dxf, dx3)
    return dx2, dconv, dg


BIG_NAMES, SMALL_NAMES, REP_NAMES = list(BIG_SHARDED), list(SMALL_SHARDED), list(REPLICATED)
SMALL_SIZES = [int(np.prod(_shard_shape(*SMALL_SHARDED[n]))) for n in SMALL_NAMES]


PIECES = [('w_in_t', 384), ('w_glu', 32), ('w_out', 128), ('pool_w', 32), ('wq0', 128), ('wq1', 128),
          ('wkv_t0', 256), ('wkv_t1', 256), ('wo0', 128), ('wo1', 128), ('up_t0', 704), ('up_t1', 704),
          ('down0', 352), ('down1', 352)]
W_IN_ROWS = 4 * WIDTH_A + 2 * N_HEADS_A + SSM_WIDTH
W_IN_PIECE = W_IN_ROWS // N_DEV


def _row_tile(rows):
    return max(t for t in range(16, min(rows, 512) + 1, 16) if rows % t == 0)


class _Riders:
    def __init__(self):
        self.waiting = {}
        self.deferred = {}
        self.grads = {}
        self.groups = []
        self.reduced = {}

    def add(self, host, comm, then):
        self.waiting.setdefault(host, []).append((comm, then))

    def after(self, marker, then):
        self.deferred.setdefault(marker, []).append(then)

    def mark(self, name, out=None):
        for cont in self.deferred.pop(name, []):
            step = cont()
            if step is not None:
                values, then = step
                out, values = lax.optimization_barrier((out, values))
                then(values)
        return out

    def run(self, name, fn, *args, **kw):
        riders = self.waiting.pop(name, [])
        if not riders:
            out = fn(*args, name=name, **kw)
        else:
            out, couts = fn(*args, name=name, comm=[c for c, _ in riders], **kw)
            for (_, then), got in zip(riders, couts):
                then(got)
        return self.mark(name, out)

    def grad(self, key, pieces):
        self.grads[key] = pieces
        for group in [g for g in self.groups if all(k in self.grads for k in g[1])]:
            self.groups.remove(group)
            self._reduce(*group)

    def _reduce(self, name, keys, pair_marker, sum_marker):
        arrays = [self.grads[k] for k in keys]
        rows = sum(a.shape[1] for a in arrays)
        tile = _row_tile(rows)
        from_sibling = _sequencer_swap(arrays, name + "_to_sibling")

        def after_swap():
            if len(arrays) == 1:
                chip_sums = _pair_sum_pieces(arrays[0], from_sibling, name + "_pair_sum", tr=tile)
            else:
                core = lax.axis_index("c")
                keep = jnp.concatenate(
                    [lax.dynamic_index_in_dim(a.reshape(4, 2, a.shape[1], PACK_COLS), core, 1, keepdims=False)
                     for a in arrays], axis=1)
                chip_sums = _pair_sum(keep, from_sibling, name + "_pair_sum", tr=tile)

            def exchange_among_chips(chip_sums):
                from_chips = _sequencer_chips(chip_sums, name + "_to_chips")

                def store(total):
                    off = 0
                    for k, a in zip(keys, arrays):
                        self.reduced[k] = total[off:off + a.shape[1]]
                        off += a.shape[1]

                self.after(sum_marker, lambda: (
                    _chip_sum(chip_sums, from_chips, name + "_chip_sum", tr=tile), store))

            return chip_sums, exchange_among_chips

        self.after(pair_marker, after_swap)


class _Weights:
    def __init__(self, inp):
        bf = lambda a: a.astype(BF16)
        local = {'w_in_t': bf(inp['w_in_ab'][0]).T, 'w_glu': bf(inp['w_glu_b'][0]), 'w_out': bf(inp['w_out_ab'][0]),
                 'pool_w': bf(inp['pool_w'][0]),
                 'small': _pack([inp[n] for n in SMALL_NAMES])}
        for l in range(2):
            local['wq%d' % l] = bf(inp['xa_wq'][l])
            local['wkv_t%d' % l] = bf(inp['xa_wkv'][l]).T
            local['wo%d' % l] = bf(inp['xa_wo'][l])
            local['up_t%d' % l] = bf(inp['ffn_w_up'][l]).T
            local['down%d' % l] = bf(inp['ffn_w_down'][l])
        self.local, self.full = local, {}

    def plan(self, keys):
        return _gather_comm([self.local[k] for k in keys])

    def gather_by_sequencer(self, keys, after, name, collective_id):
        arrays = [self.local[k] for k in keys]
        tie = (after.reshape(-1)[0] * 0.0).astype(arrays[0].dtype)
        arrays[0] = arrays[0] + tie
        self.land(keys, _sequencer_gather(arrays, name, collective_id))

    def land(self, keys, gathered):
        for k, g in zip(keys, gathered):
            if k == 'small':
                off = 0
                for n, size in zip(SMALL_NAMES, SMALL_SIZES):
                    self.full[n] = _merge_shards(g.reshape(N_DEV, -1)[:, off:off + size], *SMALL_SHARDED[n])
                    off += size
            elif k == 'pool_w':
                self.full[k] = jnp.swapaxes(g, 0, 1).reshape(len(POOL_WINDOWS), POOL_GROUP, POOL_GROUP)
            else:
                self.full[k] = g.reshape(N_DEV * g.shape[1], g.shape[2])


GATHER_FIRST = ['w_in_t', 'small']
GATHER_LAYER0 = ['w_glu', 'w_out', 'wq0', 'wkv_t0', 'wo0', 'down0', 'up_t0']
GATHER_LAYER1 = ['pool_w', 'wq1', 'wkv_t1', 'wo1', 'up_t1', 'down1']
GATHER_LAYER0_ID, GATHER_LAYER1_ID = 7, 8
GRAD_RIDES = [('g_down1', ['down1'], 'l1_ffn_up_dx', 'l1_xa_norm_bwd'),
              ('g_up1', ['up_t1'], 'l1_xa_norm_bwd', 'l0_ffn_up_dx'),
              ('g_xa1', ['wq1', 'wkv_t1', 'wo1', 'pool_w'], 'l0_ffn_up_dx', 'l0_xa_norm_bwd'),
              ('g_down0', ['down0'], 'l0_ffn_up_dx', 'l0_s5_scan_bwd'),
              ('g_l0', ['up_t0', 'wq0', 'wkv_t0', 'wo0'], 'l0_s5_cx_dx', 'l0_in_u_dx'),
              ('g_out', ['w_out', 'w_glu'], 'l0_s5_scan_bwd', 'l0_in_u_dx'),
              ('g_in', ['w_in_t'], 'l0_in_u_dx', 'adamw_pool_w')]


def _local_step(inp):
    f32_of = lambda n: inp[n].astype(F32)
    weights = _Weights(inp)
    riders = _Riders()
    riders.groups = list(GRAD_RIDES)
    full = weights.full
    weights.land(GATHER_FIRST, _comm_only(weights.plan(GATHER_FIRST), "gather_first"))
    weights.gather_by_sequencer(GATHER_LAYER0, full['w_in_t'], "gather_layer0", GATHER_LAYER0_ID)
    w_in_t = full['w_in_t']
    wts0 = dict(w_qkv_t=w_in_t[:3 * WIDTH_A], w_gate_t=w_in_t[3 * WIDTH_A:4 * WIDTH_A],
                w_ba_t=jnp.concatenate([w_in_t[4 * WIDTH_A:4 * WIDTH_A + 8], jnp.zeros((LANE - 8, D_MODEL), BF16)], 0),
                w_u_t=w_in_t[4 * WIDTH_A + 8:])
    lb_disc, disc_vjp = jax.vjp(_s5_discretise, f32_of('ssm_lambda_re')[0], f32_of('ssm_lambda_im')[0],
                                f32_of('ssm_b_re')[0], f32_of('ssm_b_im')[0], f32_of('ssm_log_dt')[0])
    b_in, c_out, a_row = _s5_matrices(*lb_disc, f32_of('ssm_c_re')[0], f32_of('ssm_c_im')[0])
    zeros4 = jnp.zeros((1, 4), F32)
    p0 = dict(conv_qkv=full['conv_qkv_a'][0], onorm_g=f32_of('onorm_g_a'),
              arow=jnp.concatenate([zeros4, f32_of('a_log_a'), jnp.zeros((1, LANE - 8), F32)], 1),
              brow=jnp.concatenate([zeros4, f32_of('dt_bias_a'), jnp.zeros((1, LANE - 8), F32)], 1),
              b_in=b_in.astype(BF16), c_out=c_out.astype(BF16), a_row=a_row,
              d_row=f32_of('ssm_d').reshape(1, SSM_WIDTH), b_glu=f32_of('b_glu_b'))

    x0 = inp['x'][0]
    mem_n = _rms_fwd(inp['mem'][0], inp['norm_mem_g'], BF16, "mem_norm")
    xn0 = _rms_fwd(x0, inp['norm_mix_g'][0], BF16, "l0_mix_norm")
    x1, sv_mix0 = _hybrid_fwd(xn0, x0, wts0, p0, weights, riders)
    x2, sv_xa0 = _xa_fwd(x1, inp['norm_xa_g'][0], mem_n, full['wq0'], full['wkv_t0'], full['wo0'], "l0_xa", riders)
    x3, sv_ffn0 = _ffn_fwd(x2, inp['norm_ffn_g'][0], full['up_t0'], full['ffn_conv'][0], full['down0'], "l0_ffn", riders)
    xn1 = _rms_fwd(x3, inp['norm_mix_g'][1], F32, "l1_mix_norm")
    x4 = _pool_fwd(xn1, full['pool_w'], full['pool_scale'], x3, "l1_pool")
    x5, sv_xa1 = _xa_fwd(x4, inp['norm_xa_g'][1], mem_n, full['wq1'], full['wkv_t1'], full['wo1'], "l1_xa", riders)
    x6, sv_ffn1 = _ffn_fwd(x5, inp['norm_ffn_g'][1], full['up_t1'], full['ffn_conv'][1], full['down1'], "l1_ffn", riders)
    loss_part, dx6, dg_final = _loss_head(x6, inp['norm_final_g'], inp['loss_target'][0], "loss_head")

    dx5, dconv1, dg_ffn1 = _ffn_bwd(dx6, x5, inp['norm_ffn_g'][1], full['up_t1'], full['ffn_conv'][1], full['down1'],
                                    sv_ffn1, "l1_ffn", 1, riders)
    dx4, dmem1, dg_xa1 = _xa_bwd(dx5, x4, inp['norm_xa_g'][1], mem_n, full['wq1'], full['wkv_t1'], full['wo1'],
                                 sv_xa1, "l1_xa", 1, riders)
    dxn1, dpool_w, dpool_scale = _pool_bwd(xn1, full['pool_w'], full['pool_scale'], dx4, "l1_pool_bwd")
    pool_pieces = jnp.swapaxes(dpool_w.astype(BF16).reshape(len(POOL_WINDOWS), N_DEV, -1, POOL_GROUP), 0, 1)
    riders.grad('pool_w', pool_pieces.reshape(N_DEV, -1, PACK_COLS))
    dx3, dg_mix1 = riders.run("l1_mix_norm_bwd", _rms_bwd, x3, inp['norm_mix_g'][1], dxn1, dx4)
    dx2, dconv0, dg_ffn0 = _ffn_bwd(dx3, x2, inp['norm_ffn_g'][0], full['up_t0'], full['ffn_conv'][0], full['down0'],
                                    sv_ffn0, "l0_ffn", 0, riders)
    dx1, dmem0, dg_xa0 = _xa_bwd(dx2, x1, inp['norm_xa_g'][0], mem_n, full['wq0'], full['wkv_t0'], full['wo0'],
                                 sv_xa0, "l0_xa", 0, riders)
    dxn0, g_mix0 = _hybrid_bwd(dx1, xn0, wts0, p0, sv_mix0, riders)
    grad_x, dg_mix0 = _rms_bwd(x0, inp['norm_mix_g'][0], dxn0, dx1, "l0_mix_norm_bwd")
    _, dg_mem = _rms_bwd(inp['mem'][0], inp['norm_mem_g'], dmem0 + dmem1, None, "mem_norm_bwd")
    assert not riders.groups and not riders.waiting and all(k.startswith("adamw_") for k in riders.deferred), (
        riders.groups, list(riders.waiting), list(riders.deferred))

    db_in, dc_out, da_row, dd = g_mix0['s5']
    dlb_re, dlb_im, dbb_re, dbb_im, dc_re, dc_im = _s5_matrix_grads(db_in, dc_out, da_row)
    dlam_re, dlam_im, dbr, dbi, dlog_dt = disc_vjp((dlb_re, dlb_im, dbb_re, dbb_im))

    rep_grads = {
        'norm_mix_g': jnp.concatenate([dg_mix0, dg_mix1], 0), 'norm_xa_g': jnp.concatenate([dg_xa0, dg_xa1], 0),
        'norm_ffn_g': jnp.concatenate([dg_ffn0, dg_ffn1], 0), 'norm_mem_g': dg_mem.reshape(-1),
        'norm_final_g': dg_final.reshape(-1), 'a_log_a': g_mix0['a_log_a'], 'dt_bias_a': g_mix0['dt_bias_a'],
        'onorm_g_a': g_mix0['onorm_g_a'], 'ssm_lambda_re': dlam_re[None], 'ssm_lambda_im': dlam_im[None],
        'ssm_b_re': dbr[None], 'ssm_b_im': dbi[None], 'ssm_c_re': dc_re[None], 'ssm_c_im': dc_im[None],
        'ssm_d': dd.reshape(1, N_GROUPS, SSM_GROUP), 'ssm_log_dt': dlog_dt[None], 'b_glu_b': g_mix0['b_glu_b']}
    small_grads = {'conv_qkv_a': g_mix0['conv_qkv_a'][None], 'pool_scale': dpool_scale,
                   'ffn_conv': jnp.stack([dconv0, dconv1])}
    return loss_part, grad_x, riders, rep_grads, small_grads


ADAMW_ORDER = ['ffn_w_up', 'ffn_w_down', 'xa_wkv', 'xa_wq', 'xa_wo', 'w_out_ab', 'w_glu_b', 'pool_w', 'w_in_ab']


def _update(inp, loss_part, grad_x, riders, rep_grads, small_grads):
    dev = _device_index()
    misc_local = _pack([rep_grads[n] for n in REP_NAMES] + [small_grads[n] for n in SMALL_NAMES] + [loss_part])
    (misc_all,) = _sequencer_gather([misc_local], "gather_small_grads", GATHER_LAYER0_ID)
    piece = lambda key: riders.reduced[key]
    both = lambda name: jnp.stack([piece(name + '0'), piece(name + '1')])
    swap = lambda a: jnp.swapaxes(a, -1, -2)
    reduced = {'w_in_ab': lambda: piece('w_in_t')[:W_IN_PIECE][None],
               'w_glu_b': lambda: piece('w_glu').reshape(inp['w_glu_b'].shape),
               'w_out_ab': lambda: piece('w_out')[None], 'pool_w': lambda: piece('pool_w').reshape(inp['pool_w'].shape),
               'xa_wq': lambda: both('wq'), 'xa_wkv': lambda: both('wkv_t'), 'xa_wo': lambda: both('wo'),
               'ffn_w_up': lambda: both('up_t'), 'ffn_w_down': lambda: both('down')}
    transposed = ('w_in_ab', 'xa_wkv', 'ffn_w_up')
    grads, upd = {}, {}
    assert sorted(ADAMW_ORDER) == sorted(BIG_NAMES)
    for n in ADAMW_ORDER:
        fix = swap if n in transposed else (lambda a: a)
        g = reduced[n]()
        out = riders.run("adamw_" + n, _adamw, fix(inp[n]), g, fix(inp['m_' + n]), fix(inp['v_' + n]))
        upd[n], grads[n] = tuple(fix(o) for o in out), fix(g)
    assert not riders.waiting and not riders.deferred, (list(riders.waiting), list(riders.deferred))
    misc_sum = _sum_leading(misc_all, "small_grads_sum")
    misc = _unpack(misc_sum, [inp[n].shape for n in REP_NAMES] + [SMALL_SHARDED[n][0] for n in SMALL_NAMES] + [()])
    loss = misc.pop()
    for n, g in zip(REP_NAMES, misc):
        grads[n] = g
    for n, g in zip(SMALL_NAMES, misc[len(REP_NAMES):]):
        grads[n] = lax.dynamic_index_in_dim(_split_shards(g, SMALL_SHARDED[n][1]), dev, 0, keepdims=False
                                            ).reshape(inp[n].shape)
    tiny_names = REP_NAMES + SMALL_NAMES
    rep_total = sum(int(np.prod(inp[n].shape)) for n in REP_NAMES)
    packs = [_pack([inp[prefix + n] for n in tiny_names]) for prefix in ('', 'm_', 'v_')]
    g_pack = _pack([misc_sum.reshape(-1)[:rep_total]] + [grads[n] for n in SMALL_NAMES])
    tiny_out = [_unpack(o, [inp[n].shape for n in tiny_names])
                for o in _adamw(packs[0], g_pack, packs[1], packs[2], "adamw_small")]
    for i, n in enumerate(tiny_names):
        upd[n] = tuple(o[i] for o in tiny_out)

    outs = [loss, grad_x[None]]
    outs += [grads[n] for n in WEIGHT_NAMES]
    for i in range(3):
        outs += [upd[n][i] for n in WEIGHT_NAMES]
    return tuple(outs)


def _step(inp):
    loss_part, grad_x, riders, rep_grads, small_grads = _local_step(inp)
    return _update(inp, loss_part, grad_x, riders, rep_grads, small_grads)


INPUT_NAMES = (['x', 'mem'] + WEIGHT_NAMES + ['loss_target'] + ['m_' + n for n in WEIGHT_NAMES]
               + ['v_' + n for n in WEIGHT_NAMES])


def kernel(x, mem, norm_mix_g, norm_xa_g, norm_ffn_g, norm_mem_g, norm_final_g, w_in_ab, conv_qkv_a, a_log_a, dt_bias_a, onorm_g_a, ssm_lambda_re, ssm_lambda_im, ssm_b_re, ssm_b_im, ssm_c_re, ssm_c_im, ssm_d, ssm_log_dt, w_glu_b, b_glu_b, w_out_ab, pool_w, pool_scale, xa_wq, xa_wkv, xa_wo, ffn_w_up, ffn_conv, ffn_w_down, loss_target, m_norm_mix_g, m_norm_xa_g, m_norm_ffn_g, m_norm_mem_g, m_norm_final_g, m_w_in_ab, m_conv_qkv_a, m_a_log_a, m_dt_bias_a, m_onorm_g_a, m_ssm_lambda_re, m_ssm_lambda_im, m_ssm_b_re, m_ssm_b_im, m_ssm_c_re, m_ssm_c_im, m_ssm_d, m_ssm_log_dt, m_w_glu_b, m_b_glu_b, m_w_out_ab, m_pool_w, m_pool_scale, m_xa_wq, m_xa_wkv, m_xa_wo, m_ffn_w_up, m_ffn_conv, m_ffn_w_down, v_norm_mix_g, v_norm_xa_g, v_norm_ffn_g, v_norm_mem_g, v_norm_final_g, v_w_in_ab, v_conv_qkv_a, v_a_log_a, v_dt_bias_a, v_onorm_g_a, v_ssm_lambda_re, v_ssm_lambda_im, v_ssm_b_re, v_ssm_b_im, v_ssm_c_re, v_ssm_c_im, v_ssm_d, v_ssm_log_dt, v_w_glu_b, v_b_glu_b, v_w_out_ab, v_pool_w, v_pool_scale, v_xa_wq, v_xa_wkv, v_xa_wo, v_ffn_w_up, v_ffn_conv, v_ffn_w_down):
    args = (x, mem, norm_mix_g, norm_xa_g, norm_ffn_g, norm_mem_g, norm_final_g, w_in_ab, conv_qkv_a, a_log_a, dt_bias_a, onorm_g_a, ssm_lambda_re, ssm_lambda_im, ssm_b_re, ssm_b_im, ssm_c_re, ssm_c_im, ssm_d, ssm_log_dt, w_glu_b, b_glu_b, w_out_ab, pool_w, pool_scale, xa_wq, xa_wkv, xa_wo, ffn_w_up, ffn_conv, ffn_w_down, loss_target, m_norm_mix_g, m_norm_xa_g, m_norm_ffn_g, m_norm_mem_g, m_norm_final_g, m_w_in_ab, m_conv_qkv_a, m_a_log_a, m_dt_bias_a, m_onorm_g_a, m_ssm_lambda_re, m_ssm_lambda_im, m_ssm_b_re, m_ssm_b_im, m_ssm_c_re, m_ssm_c_im, m_ssm_d, m_ssm_log_dt, m_w_glu_b, m_b_glu_b, m_w_out_ab, m_pool_w, m_pool_scale, m_xa_wq, m_xa_wkv, m_xa_wo, m_ffn_w_up, m_ffn_conv, m_ffn_w_down, v_norm_mix_g, v_norm_xa_g, v_norm_ffn_g, v_norm_mem_g, v_norm_final_g, v_w_in_ab, v_conv_qkv_a, v_a_log_a, v_dt_bias_a, v_onorm_g_a, v_ssm_lambda_re, v_ssm_lambda_im, v_ssm_b_re, v_ssm_b_im, v_ssm_c_re, v_ssm_c_im, v_ssm_d, v_ssm_log_dt, v_w_glu_b, v_b_glu_b, v_w_out_ab, v_pool_w, v_pool_scale, v_xa_wq, v_xa_wkv, v_xa_wo, v_ffn_w_up, v_ffn_conv, v_ffn_w_down)
    return _step(dict(zip(INPUT_NAMES, args)))
```

```python
import functools
import math

import numpy as np
import jax
import jax.numpy as jnp
from jax import lax
from jax.experimental import pallas as pl
from jax.experimental.pallas import tpu as pltpu
from jax.experimental.pallas import tpu_sc as plsc

F32, BF16 = jnp.float32, jnp.bfloat16
HIGH, HIGHEST = lax.Precision.HIGH, lax.Precision.HIGHEST
MESH = pl.DeviceIdType.MESH

N_DEV = 8
SEQ, D_MODEL, MEM_LEN = 2048, 1024, 256
WIDTH_A, N_HEADS_A, HEAD_A, CONV_A = 512, 4, 128, 4
GDR_CHUNK = 128
SSM_WIDTH, SSM_GROUP, N_GROUPS, SSM_STATE = 512, 16, 32, 64
SSM_CH = N_GROUPS * SSM_STATE
SCAN_CB = 512
SCAN_UNROLL = 4
POOL_WINDOWS = (2, 4, 8, 16)
POOL_GROUP = 256
N_HEADS_X, HEAD_X = 4, 256
D_FF, CONV_FFN = 2816, 3
RMS_EPS = 1e-6
ADAM_LR, ADAM_B1, ADAM_B2, ADAM_EPS, ADAM_WD, ADAM_STEP = 0.001, 0.9, 0.999, 1e-08, 0.01, 10
LANE = 128
PACK_COLS = 1024
VMEM_LIMIT_BYTES = 56 * 1024 * 1024


def _params(sem=None):
    return pltpu.CompilerParams(dimension_semantics=sem, vmem_limit_bytes=VMEM_LIMIT_BYTES)


class Comm:
    def __init__(self, inputs, out_shapes, sems, start, end, mid=None):
        self.inputs, self.out_shapes, self.sems = list(inputs), list(out_shapes), list(sems)
        self.start, self.mid, self.end = start, mid, end


def _merge_comms(comms):
    comms = [c for c in comms if c is not None]
    if not comms:
        return None, []
    bounds, ni, no, ns = [], 0, 0, 0
    for c in comms:
        bounds.append((ni, no, ns))
        ni, no, ns = ni + len(c.inputs), no + len(c.out_shapes), ns + len(c.sems)

    def phase(which):
        def run(ins, outs, sems):
            for c, (i0, o0, s0) in zip(comms, bounds):
                fn = getattr(c, which)
                if fn is not None:
                    fn(ins[i0:i0 + len(c.inputs)], outs[o0:o0 + len(c.out_shapes)], sems[s0:s0 + len(c.sems)])
        return run

    merged = Comm([a for c in comms for a in c.inputs], [s for c in comms for s in c.out_shapes],
                  [s for c in comms for s in c.sems], phase("start"), phase("end"), phase("mid"))
    return merged, [(o0, o0 + len(c.out_shapes)) for c, (_, o0, _) in zip(comms, bounds)]


def _call(body, *, name, grid, in_specs, out_specs, out_shape, args, scratch_shapes=(), sem=None, comm=None):
    single = not isinstance(out_shape, (list, tuple))
    out_specs_l = [out_specs] if single else list(out_specs)
    out_shape_l = [out_shape] if single else list(out_shape)
    scratch_shapes = list(scratch_shapes)
    merged, spans = _merge_comms(comm if isinstance(comm, (list, tuple)) else [comm])
    if merged is None:
        outs = pl.pallas_call(body, name=name, grid=grid, in_specs=list(in_specs), out_specs=out_specs_l,
                              out_shape=out_shape_l, scratch_shapes=scratch_shapes, compiler_params=_params(sem))(*args)
        outs = outs[0] if single else outs
        return outs if comm is None else (outs, [])
    n_in, n_out, n_scr = len(in_specs), len(out_specs_l), len(scratch_shapes)
    ci, co = len(merged.inputs), len(merged.out_shapes)
    total = int(np.prod(grid))

    def wrapped(*refs):
        ins, cins = refs[:n_in], refs[n_in:n_in + ci]
        outs, couts = refs[n_in + ci:n_in + ci + n_out], refs[n_in + ci + n_out:n_in + ci + n_out + co]
        scr, csems = refs[n_in + ci + n_out + co:n_in + ci + n_out + co + n_scr], refs[n_in + ci + n_out + co + n_scr:]
        lin = pl.program_id(0)
        for d in range(1, len(grid)):
            lin = lin * grid[d] + pl.program_id(d)
        pl.when(lin == 0)(lambda: merged.start(cins, couts, csems))
        body(*ins, *outs, *scr)
        mid_step = min((3 * total) // 4, total - 1)
        pl.when(lin == mid_step)(lambda: merged.mid(cins, couts, csems))
        pl.when(lin == total - 1)(lambda: merged.end(cins, couts, csems))

    any_spec = pl.BlockSpec(memory_space=pl.ANY)
    res = pl.pallas_call(
        wrapped, name=name, grid=grid, in_specs=list(in_specs) + [any_spec] * ci,
        out_specs=out_specs_l + [any_spec] * co, out_shape=out_shape_l + merged.out_shapes,
        scratch_shapes=scratch_shapes + merged.sems,
        compiler_params=_params(("arbitrary",) * len(grid)))(*args, *merged.inputs)
    outs, couts = res[:n_out], res[n_out:]
    return (outs[0] if single else list(outs)), [list(couts[a:b]) for a, b in spans]


def _comm_only(comm, name):
    def body():
        pass

    _, couts = _call(body, name=name, grid=(1,), in_specs=[], out_specs=[], out_shape=[], args=[], comm=comm)
    return couts[0]


def _tile(dim, pref):
    best = None
    for t in range(LANE, min(dim, pref) + 1, LANE):
        if dim % t == 0:
            best = t
    return best if best is not None else dim


MM_VMEM_BUDGET = 40 * 1024 * 1024


def _mm_tiles(m, n, k, a_bytes, b_bytes, o_bytes, r_bytes):
    for tk in (k, _tile(k, 2048), _tile(k, 1024), _tile(k, 512)):
        for tm, tn in ((1024, 1536), (1024, 1024), (1024, 512), (512, 512), (256, 512), (256, 256)):
            tm, tn = _tile(m, tm), _tile(n, tn)
            acc = 0 if tk == k else tm * tn * 4
            need = 2 * (tm * tk * a_bytes + tk * tn * b_bytes + tm * tn * (o_bytes + r_bytes)) + acc
            if need <= MM_VMEM_BUDGET:
                return tm, tn, tk
    raise ValueError("no matmul tiling fits VMEM")


def _mm(a, b, mode, name, out_dtype=F32, res=None, comm=None):
    if mode == "nn":
        (m, k), n = a.shape, b.shape[1]
    elif mode == "nt":
        (m, k), n = a.shape, b.shape[0]
    else:
        (k, m), n = a.shape, b.shape[1]
    tm, tn, tk = _mm_tiles(m, n, k, a.dtype.itemsize, b.dtype.itemsize, jnp.dtype(out_dtype).itemsize,
                           0 if res is None else res.dtype.itemsize)
    nk = k // tk
    dims = {"nn": ((1,), (0,)), "nt": ((1,), (1,)), "tn": ((0,), (0,))}[mode]

    def body(*refs):
        if res is None:
            a_ref, b_ref, o_ref = refs[:3]
            r_ref = None
        else:
            a_ref, b_ref, r_ref, o_ref = refs[:4]
        part = lax.dot_general(a_ref[...].astype(BF16), b_ref[...].astype(BF16), (dims, ((), ())),
                               preferred_element_type=F32)

        def finish(out):
            if r_ref is not None:
                out = out + r_ref[...].astype(F32)
            o_ref[...] = out.astype(out_dtype)

        if nk == 1:
            finish(part)
            return
        acc = refs[-1]
        kk = pl.program_id(2)

        @pl.when(kk == 0)
        def _():
            acc[...] = part

        @pl.when(kk > 0)
        def _():
            acc[...] += part

        @pl.when(kk == nk - 1)
        def _():
            finish(acc[...])

    a_spec = (pl.BlockSpec((tk, tm), lambda i, j, q: (q, i)) if mode == "tn"
              else pl.BlockSpec((tm, tk), lambda i, j, q: (i, q)))
    b_spec = (pl.BlockSpec((tn, tk), lambda i, j, q: (j, q)) if mode == "nt"
              else pl.BlockSpec((tk, tn), lambda i, j, q: (q, j)))
    o_spec = pl.BlockSpec((tm, tn), lambda i, j, q: (i, j))
    in_specs, args = [a_spec, b_spec], [a, b]
    if res is not None:
        in_specs.append(o_spec)
        args.append(res)
    return _call(body, name=name, grid=(m // tm, n // tn, nk), in_specs=in_specs, out_specs=o_spec,
                 out_shape=jax.ShapeDtypeStruct((m, n), out_dtype),
                 scratch_shapes=[] if nk == 1 else [pltpu.VMEM((tm, tn), F32)],
                 sem=("parallel", "parallel", "arbitrary"), args=args, comm=comm)


def _mm_bd(a, b, mode, name, out_dtype=F32, res=None, comm=None, tm=1024):
    if mode == "tn":
        k = a.shape[0]
        nb = min(a.shape[1], b.shape[1]) // LANE
        ma, n = a.shape[1] // nb, b.shape[1] // nb

        def body(a_ref, b_ref, o_ref):
            o_ref[0] = lax.dot_general(a_ref[...].astype(BF16), b_ref[...].astype(BF16), (((0,), (0,)), ((), ())),
                                       preferred_element_type=F32).astype(out_dtype)

        return _call(body, name=name, grid=(nb,),
                     in_specs=[pl.BlockSpec((k, ma), lambda j: (0, j)), pl.BlockSpec((k, n), lambda j: (0, j))],
                     out_specs=pl.BlockSpec((1, ma, n), lambda j: (j, 0, 0)),
                     out_shape=jax.ShapeDtypeStruct((nb, ma, n), out_dtype), sem=("parallel",), args=(a, b), comm=comm)
    m = a.shape[0]
    nb = b.shape[0]
    ka = a.shape[1] // nb
    n = b.shape[2] if mode == "nn" else b.shape[1]
    tm = _tile(m, tm)
    dims = ((1,), (0,)) if mode == "nn" else ((1,), (1,))

    def body(*refs):
        if res is None:
            a_ref, b_ref, o_ref = refs
            r_ref = None
        else:
            a_ref, b_ref, r_ref, o_ref = refs
        out = lax.dot_general(a_ref[...].astype(BF16), b_ref[0].astype(BF16), (dims, ((), ())),
                              preferred_element_type=F32)
        if r_ref is not None:
            out = out + r_ref[...].astype(F32)
        o_ref[...] = out.astype(out_dtype)

    o_spec = pl.BlockSpec((tm, n), lambda i, j: (i, j))
    in_specs = [pl.BlockSpec((tm, ka), lambda i, j: (i, j)), pl.BlockSpec((1,) + b.shape[1:], lambda i, j: (j, 0, 0))]
    args = [a, b]
    if res is not None:
        in_specs.append(o_spec)
        args.append(res)
    return _call(body, name=name, grid=(m // tm, nb), in_specs=in_specs, out_specs=o_spec,
                 out_shape=jax.ShapeDtypeStruct((m, nb * n), out_dtype), sem=("parallel", "parallel"),
                 args=args, comm=comm)


def _rms_fwd(x, g, out_dtype, name, tr=256):
    rows, d = x.shape

    def body(x_ref, g_ref, o_ref):
        xv = x_ref[...]
        r = lax.rsqrt(jnp.mean(xv * xv, axis=-1, keepdims=True) + RMS_EPS)
        o_ref[...] = (xv * r * g_ref[...]).astype(out_dtype)

    return pl.pallas_call(
        body, name=name, grid=(rows // tr,),
        in_specs=[pl.BlockSpec((tr, d), lambda i: (i, 0)), pl.BlockSpec((1, d), lambda i: (0, 0))],
        out_specs=pl.BlockSpec((tr, d), lambda i: (i, 0)), out_shape=jax.ShapeDtypeStruct((rows, d), out_dtype),
        compiler_params=_params(("parallel",)))(x, g.reshape(1, d))


def _rms_bwd(x, g, dy, dres, name, tr=256, comm=None):
    rows, d = x.shape

    def body(*refs):
        if dres is None:
            x_ref, g_ref, dy_ref, dx_ref, dg_ref = refs
            r_ref = None
        else:
            x_ref, g_ref, dy_ref, r_ref, dx_ref, dg_ref = refs

        @pl.when(pl.program_id(0) == 0)
        def _():
            dg_ref[...] = jnp.zeros_like(dg_ref)

        xv, dyv = x_ref[...], dy_ref[...].astype(F32)
        r = lax.rsqrt(jnp.mean(xv * xv, axis=-1, keepdims=True) + RMS_EPS)
        xh = xv * r
        dyg = dyv * g_ref[...]
        dx = r * (dyg - xh * jnp.mean(dyg * xh, axis=-1, keepdims=True))
        if r_ref is not None:
            dx = dx + r_ref[...]
        dx_ref[...] = dx
        dg_ref[...] += jnp.sum(dyv * xh, axis=0, keepdims=True)

    blk = pl.BlockSpec((tr, d), lambda i: (i, 0))
    vec = pl.BlockSpec((1, d), lambda i: (0, 0))
    in_specs, args = [blk, vec, blk], [x, g.reshape(1, d), dy]
    if dres is not None:
        in_specs.append(blk)
        args.append(dres)
    return _call(
        body, name=name, grid=(rows // tr,), in_specs=in_specs, out_specs=[blk, vec],
        out_shape=[jax.ShapeDtypeStruct((rows, d), F32), jax.ShapeDtypeStruct((1, d), F32)],
        sem=("arbitrary",), args=args, comm=comm)


def _loss_head(x, g, target, name, tr=256):
    rows, d = x.shape

    def body(x_ref, g_ref, t_ref, loss_ref, dx_ref, dg_ref):
        @pl.when(pl.program_id(0) == 0)
        def _():
            dg_ref[...] = jnp.zeros_like(dg_ref)
            loss_ref[...] = jnp.zeros_like(loss_ref)

        xv = x_ref[...]
        r = lax.rsqrt(jnp.mean(xv * xv, axis=-1, keepdims=True) + RMS_EPS)
        xh = xv * r
        err = xh * g_ref[...] - t_ref[...]
        loss_ref[...] += 0.5 * jnp.sum(jnp.mean(err * err, axis=-1, keepdims=True), keepdims=True)
        dyv = err * (1.0 / d)
        dyg = dyv * g_ref[...]
        dx_ref[...] = r * (dyg - xh * jnp.mean(dyg * xh, axis=-1, keepdims=True))
        dg_ref[...] += jnp.sum(dyv * xh, axis=0, keepdims=True)

    blk = pl.BlockSpec((tr, d), lambda i: (i, 0))
    vec = pl.BlockSpec((1, d), lambda i: (0, 0))
    return pl.pallas_call(
        body, name=name, grid=(rows // tr,), in_specs=[blk, vec, blk],
        out_specs=[pl.BlockSpec((1, 1), lambda i: (0, 0)), blk, vec],
        out_shape=[jax.ShapeDtypeStruct((1, 1), F32), jax.ShapeDtypeStruct((rows, d), F32),
                   jax.ShapeDtypeStruct((1, d), F32)],
        compiler_params=_params(("arbitrary",)))(x, g.reshape(1, d), target)


def _shift_down(x, s):
    rows = lax.broadcasted_iota(jnp.int32, x.shape, 0)
    return jnp.where(rows >= s, pltpu.roll(x, s, 0), 0.0)


def _shift_up(x, s):
    n = x.shape[0]
    rows = lax.broadcasted_iota(jnp.int32, x.shape, 0)
    return jnp.where(rows < n - s, pltpu.roll(x, n - s, 0), 0.0)


def _sigmoid(x):
    return 1.0 / (1.0 + jnp.exp(-x))


def _silu_and_grad(x):
    s = _sigmoid(x)
    return x * s, s * (1.0 + x * (1.0 - s))


_GELU_C0, _GELU_C1 = math.sqrt(2.0 / math.pi), 0.044715


def _gelu_and_grad(x):
    th = jnp.tanh(_GELU_C0 * (x + _GELU_C1 * x * x * x))
    y = 0.5 * x * (1.0 + th)
    dy = 0.5 * (1.0 + th) + 0.5 * x * (1.0 - th * th) * _GELU_C0 * (1.0 + 3.0 * _GELU_C1 * x * x)
    return y, dy


def _ffn_act_fwd(h, w, name, tc=256, comm=None):
    t = h.shape[0]
    nb = D_FF // tc

    def body(hg_ref, hv_ref, wg_ref, wv_ref, a_ref):
        def conv(x, wr):
            return wr[2:3, :] * x + wr[1:2, :] * _shift_down(x, 1) + wr[0:1, :] * _shift_down(x, 2)

        cg = conv(hg_ref[...], wg_ref[...])
        cv = conv(hv_ref[...], wv_ref[...])
        a_ref[...] = (cg * _sigmoid(cg) * cv).astype(BF16)

    return _call(
        body, name=name, grid=(nb,),
        in_specs=[pl.BlockSpec((t, tc), lambda j: (0, j)), pl.BlockSpec((t, tc), lambda j: (0, j + nb)),
                  pl.BlockSpec((CONV_FFN, tc), lambda j: (0, j)), pl.BlockSpec((CONV_FFN, tc), lambda j: (0, j + nb))],
        out_specs=pl.BlockSpec((t, tc), lambda j: (0, j)), out_shape=jax.ShapeDtypeStruct((t, D_FF), BF16),
        sem=("parallel",), args=(h, h, w, w), comm=comm)


def _ffn_act_bwd(h, w, da, name, tc=256, comm=None):
    t = h.shape[0]
    nb = D_FF // tc

    def body(hg_ref, hv_ref, wg_ref, wv_ref, da_ref, dhg_ref, dhv_ref, dwg_ref, dwv_ref):
        hg, hv, wg, wv = hg_ref[...], hv_ref[...], wg_ref[...], wv_ref[...]
        hg1, hg2, hv1, hv2 = _shift_down(hg, 1), _shift_down(hg, 2), _shift_down(hv, 1), _shift_down(hv, 2)
        cg = wg[2:3, :] * hg + wg[1:2, :] * hg1 + wg[0:1, :] * hg2
        cv = wv[2:3, :] * hv + wv[1:2, :] * hv1 + wv[0:1, :] * hv2
        sg, dsg = _silu_and_grad(cg)
        dav = da_ref[...].astype(F32)
        dcv = dav * sg
        dcg = dav * cv * dsg

        def conv_t(dc, wr):
            return wr[2:3, :] * dc + wr[1:2, :] * _shift_up(dc, 1) + wr[0:1, :] * _shift_up(dc, 2)

        dhg_ref[...] = conv_t(dcg, wg).astype(BF16)
        dhv_ref[...] = conv_t(dcv, wv).astype(BF16)
        dwg_ref[0:1, :] = jnp.sum(dcg * hg2, axis=0, keepdims=True)
        dwg_ref[1:2, :] = jnp.sum(dcg * hg1, axis=0, keepdims=True)
        dwg_ref[2:3, :] = jnp.sum(dcg * hg, axis=0, keepdims=True)
        dwv_ref[0:1, :] = jnp.sum(dcv * hv2, axis=0, keepdims=True)
        dwv_ref[1:2, :] = jnp.sum(dcv * hv1, axis=0, keepdims=True)
        dwv_ref[2:3, :] = jnp.sum(dcv * hv, axis=0, keepdims=True)

    big = lambda off: pl.BlockSpec((t, tc), lambda j: (0, j + off))
    small = lambda off: pl.BlockSpec((CONV_FFN, tc), lambda j: (0, j + off))
    res = _call(
        body, name=name, grid=(nb,),
        in_specs=[big(0), big(nb), small(0), small(nb), big(0)],
        out_specs=[big(0), big(0), small(0), small(0)],
        out_shape=[jax.ShapeDtypeStruct((t, D_FF), BF16), jax.ShapeDtypeStruct((t, D_FF), BF16),
                   jax.ShapeDtypeStruct((CONV_FFN, D_FF), F32), jax.ShapeDtypeStruct((CONV_FFN, D_FF), F32)],
        sem=("parallel",), args=(h, h, w, w, da), comm=comm)
    (dhg, dhv, dwg, dwv), couts = res if comm is not None else (res, None)
    out = (jnp.concatenate([dhg, dhv], axis=1), jnp.concatenate([dwg, dwv], axis=1))
    return out if comm is None else (out, couts)


def _attn_probs(q, k):
    s = lax.dot_general(q.astype(BF16), k.astype(BF16), (((1,), (1,)), ((), ())),
                        preferred_element_type=F32) * (HEAD_X ** -0.5)
    s = s - jnp.max(s, axis=-1, keepdims=True)
    p = jnp.exp(s)
    return p / jnp.sum(p, axis=-1, keepdims=True)


def _attn_fwd(q, kv, name, tq=512, comm=None):
    t = q.shape[0]

    def body(q_ref, k_ref, v_ref, o_ref):
        p = _attn_probs(q_ref[...], k_ref[...])
        o_ref[...] = jnp.dot(p.astype(BF16), v_ref[...].astype(BF16), preferred_element_type=F32).astype(BF16)

    return _call(
        body, name=name, grid=(N_HEADS_X, t // tq),
        in_specs=[pl.BlockSpec((tq, HEAD_X), lambda h, i: (i, h)),
                  pl.BlockSpec((MEM_LEN, HEAD_X), lambda h, i: (0, h)),
                  pl.BlockSpec((MEM_LEN, HEAD_X), lambda h, i: (0, h + N_HEADS_X))],
        out_specs=pl.BlockSpec((tq, HEAD_X), lambda h, i: (i, h)),
        out_shape=jax.ShapeDtypeStruct((t, N_HEADS_X * HEAD_X), BF16),
        sem=("parallel", "parallel"), args=(q, kv, kv), comm=comm)


def _attn_bwd(q, kv, do, name, tq=512):
    t = q.shape[0]

    def body(q_ref, k_ref, v_ref, do_ref, dq_ref, dk_ref, dv_ref):
        @pl.when(pl.program_id(1) == 0)
        def _():
            dk_ref[...] = jnp.zeros_like(dk_ref)
            dv_ref[...] = jnp.zeros_like(dv_ref)

        qb, kb, vb, dob = (r[...].astype(BF16) for r in (q_ref, k_ref, v_ref, do_ref))
        p = _attn_probs(qb, kb)
        dp = lax.dot_general(dob, vb, (((1,), (1,)), ((), ())), preferred_element_type=F32)
        ds = p * (dp - jnp.sum(dp * p, axis=-1, keepdims=True)) * (HEAD_X ** -0.5)
        dsb = ds.astype(BF16)
        dq_ref[...] = jnp.dot(dsb, kb, preferred_element_type=F32).astype(BF16)
        dk_ref[...] += lax.dot_general(dsb, qb, (((0,), (0,)), ((), ())), preferred_element_type=F32)
        dv_ref[...] += lax.dot_general(p.astype(BF16), dob, (((0,), (0,)), ((), ())), preferred_element_type=F32)

    qs = pl.BlockSpec((tq, HEAD_X), lambda h, i: (i, h))
    ms = pl.BlockSpec((MEM_LEN, HEAD_X), lambda h, i: (0, h))
    return pl.pallas_call(
        body, name=name, grid=(N_HEADS_X, t // tq),
        in_specs=[qs, ms, pl.BlockSpec((MEM_LEN, HEAD_X), lambda h, i: (0, h + N_HEADS_X)), qs],
        out_specs=[qs, ms, ms],
        out_shape=[jax.ShapeDtypeStruct((t, D_MODEL), BF16), jax.ShapeDtypeStruct((MEM_LEN, D_MODEL), F32),
                   jax.ShapeDtypeStruct((MEM_LEN, D_MODEL), F32)],
        compiler_params=_params(("parallel", "arbitrary")))(q, kv, kv, do)


def _pool_counts(t, win):
    pos = lax.broadcasted_iota(jnp.int32, (t, 1), 0).astype(F32) + 1.0
    return 1.0 / jnp.minimum(pos, float(win))


def _pool_delta(xv, win):
    s, step = xv, 1
    while step < win:
        s = s + _shift_down(s, step)
        step *= 2
    return s * _pool_counts(xv.shape[0], win) - xv


def _pool_delta_t(dv, win):
    s, step = dv * _pool_counts(dv.shape[0], win), 1
    while step < win:
        s = s + _shift_up(s, step)
        step *= 2
    return s - dv


def _pool_fwd(xn, w, scale, res, name):
    t = xn.shape[0]

    def make_branch(win, xn_ref, w_ref, s_ref, r_ref, o_ref):
        def branch():
            dl = _pool_delta(xn_ref[...], win)
            y = jnp.dot(dl.astype(BF16), w_ref[0], preferred_element_type=F32)
            o_ref[...] = r_ref[...] + y * s_ref[...]
        return branch

    def body(xn_ref, w_ref, s_ref, r_ref, o_ref):
        for gi, win in enumerate(POOL_WINDOWS):
            pl.when(pl.program_id(0) == gi)(make_branch(win, xn_ref, w_ref, s_ref, r_ref, o_ref))

    blk = pl.BlockSpec((t, POOL_GROUP), lambda g: (0, g))
    return pl.pallas_call(
        body, name=name, grid=(len(POOL_WINDOWS),),
        in_specs=[blk, pl.BlockSpec((1, POOL_GROUP, POOL_GROUP), lambda g: (g, 0, 0)),
                  pl.BlockSpec((1, POOL_GROUP), lambda g: (0, g)), blk],
        out_specs=blk, out_shape=jax.ShapeDtypeStruct((t, D_MODEL), F32),
        compiler_params=_params(("parallel",)))(xn, w, scale, res)


def _pool_bwd(xn, w, scale, dmix, name):
    t = xn.shape[0]

    def make_branch(win, xn_ref, w_ref, s_ref, d_ref, dxn_ref, dw_ref, ds_ref):
        def branch():
            dl = _pool_delta(xn_ref[...], win).astype(BF16)
            wv = w_ref[0]
            dm = d_ref[...]
            y = jnp.dot(dl, wv, preferred_element_type=F32)
            ds_ref[...] = jnp.sum(dm * y, axis=0, keepdims=True)
            dy = (dm * s_ref[...]).astype(BF16)
            dw_ref[0] = lax.dot_general(dl, dy, (((0,), (0,)), ((), ())), preferred_element_type=F32)
            ddl = lax.dot_general(dy, wv, (((1,), (1,)), ((), ())), preferred_element_type=F32)
            dxn_ref[...] = _pool_delta_t(ddl, win)
        return branch

    def body(*refs):
        for gi, win in enumerate(POOL_WINDOWS):
            pl.when(pl.program_id(0) == gi)(make_branch(win, *refs))

    blk = pl.BlockSpec((t, POOL_GROUP), lambda g: (0, g))
    wspec = pl.BlockSpec((1, POOL_GROUP, POOL_GROUP), lambda g: (g, 0, 0))
    vec = pl.BlockSpec((1, POOL_GROUP), lambda g: (0, g))
    return pl.pallas_call(
        body, name=name, grid=(len(POOL_WINDOWS),), in_specs=[blk, wspec, vec, blk], out_specs=[blk, wspec, vec],
        out_shape=[jax.ShapeDtypeStruct((t, D_MODEL), F32),
                   jax.ShapeDtypeStruct((len(POOL_WINDOWS), POOL_GROUP, POOL_GROUP), F32),
                   jax.ShapeDtypeStruct((1, D_MODEL), F32)],
        compiler_params=_params(("parallel",)))(xn, w, scale, dmix)


def _qkv_conv(h, wr):
    return (wr[3:4, :] * h + wr[2:3, :] * _shift_down(h, 1) + wr[1:2, :] * _shift_down(h, 2)
            + wr[0:1, :] * _shift_down(h, 3))


def _qkv_pre_fwd(h, w, col0, ncols, normalize, scale, name):
    t = h.shape[0]

    def body(h_ref, w_ref, o_ref):
        c = _qkv_conv(h_ref[...], w_ref[...])
        s = c * _sigmoid(c)
        if normalize:
            s = s * lax.rsqrt(jnp.sum(s * s, axis=-1, keepdims=True) + 1e-6) * scale
        o_ref[...] = s

    return pl.pallas_call(
        body, name=name, grid=(ncols,),
        in_specs=[pl.BlockSpec((t, HEAD_A), lambda j: (0, j + col0)), pl.BlockSpec((CONV_A, HEAD_A), lambda j: (0, j + col0))],
        out_specs=pl.BlockSpec((t, HEAD_A), lambda j: (0, j)), out_shape=jax.ShapeDtypeStruct((t, ncols * HEAD_A), F32),
        compiler_params=_params(("parallel",)))(h, w)


def _qkv_pre_bwd(h, w, dy, col0, ncols, normalize, scale, name):
    t = h.shape[0]

    def body(h_ref, w_ref, dy_ref, dh_ref, dw_ref):
        hv, wr, dyv = h_ref[...], w_ref[...], dy_ref[...]
        h1, h2, h3 = _shift_down(hv, 1), _shift_down(hv, 2), _shift_down(hv, 3)
        c = wr[3:4, :] * hv + wr[2:3, :] * h1 + wr[1:2, :] * h2 + wr[0:1, :] * h3
        s, dsilu = _silu_and_grad(c)
        if normalize:
            r = lax.rsqrt(jnp.sum(s * s, axis=-1, keepdims=True) + 1e-6)
            y = s * r
            dyv = dyv * scale
            ds = r * (dyv - y * jnp.sum(dyv * y, axis=-1, keepdims=True))
        else:
            ds = dyv
        dc = ds * dsilu
        dh = (wr[3:4, :] * dc + wr[2:3, :] * _shift_up(dc, 1) + wr[1:2, :] * _shift_up(dc, 2)
              + wr[0:1, :] * _shift_up(dc, 3))
        dh_ref[...] = dh.astype(BF16)
        dw_ref[0:1, :] = jnp.sum(dc * h3, axis=0, keepdims=True)
        dw_ref[1:2, :] = jnp.sum(dc * h2, axis=0, keepdims=True)
        dw_ref[2:3, :] = jnp.sum(dc * h1, axis=0, keepdims=True)
        dw_ref[3:4, :] = jnp.sum(dc * hv, axis=0, keepdims=True)

    return pl.pallas_call(
        body, name=name, grid=(ncols,),
        in_specs=[pl.BlockSpec((t, HEAD_A), lambda j: (0, j + col0)), pl.BlockSpec((CONV_A, HEAD_A), lambda j: (0, j + col0)),
                  pl.BlockSpec((t, HEAD_A), lambda j: (0, j))],
        out_specs=[pl.BlockSpec((t, HEAD_A), lambda j: (0, j)), pl.BlockSpec((CONV_A, HEAD_A), lambda j: (0, j))],
        out_shape=[jax.ShapeDtypeStruct((t, ncols * HEAD_A), BF16), jax.ShapeDtypeStruct((CONV_A, ncols * HEAD_A), F32)],
        compiler_params=_params(("parallel",)))(h, w, dy)


def _softplus(x):
    return jnp.maximum(x, 0.0) + jnp.log1p(jnp.exp(-jnp.abs(x)))


def _gates_fwd(ba, arow, brow, name):
    t = ba.shape[0]

    def body(x_ref, a_ref, b_ref, o_ref):
        xv = x_ref[...]
        lane = lax.broadcasted_iota(jnp.int32, xv.shape, 1)
        beta = _sigmoid(xv)
        g = -jnp.exp(a_ref[...]) * _softplus(xv + b_ref[...])
        o_ref[...] = jnp.where(lane < N_HEADS_A, beta, jnp.where(lane < 2 * N_HEADS_A, g, 0.0))

    return pl.pallas_call(body, name=name, out_shape=jax.ShapeDtypeStruct((t, LANE), F32),
                          compiler_params=_params())(ba, arow, brow)


def _gates_bwd(ba, arow, brow, dgb, name):
    t = ba.shape[0]

    def body(x_ref, a_ref, b_ref, d_ref, dx_ref, da_ref, db_ref):
        xv = x_ref[...]
        dv = d_ref[0] + d_ref[1] + d_ref[2] + d_ref[3]
        lane = lax.broadcasted_iota(jnp.int32, xv.shape, 1)
        beta = _sigmoid(xv)
        ea = jnp.exp(a_ref[...])
        z = xv + b_ref[...]
        dgv = jnp.where((lane >= N_HEADS_A) & (lane < 2 * N_HEADS_A), dv, 0.0) * (-ea)
        dz = dgv * _sigmoid(z)
        dx = jnp.where(lane < N_HEADS_A, dv * beta * (1.0 - beta), dz)
        dx_ref[...] = dx.astype(BF16)
        db_ref[...] = jnp.sum(dz, axis=0, keepdims=True)
        da_ref[...] = jnp.sum(dgv * _softplus(z), axis=0, keepdims=True)

    return pl.pallas_call(
        body, name=name,
        out_shape=[jax.ShapeDtypeStruct((t, LANE), BF16), jax.ShapeDtypeStruct((1, LANE), F32),
                   jax.ShapeDtypeStruct((1, LANE), F32)],
        compiler_params=_params())(ba, arow, brow, dgb)


def _head_gates(gates, head):
    lane = lax.broadcasted_iota(jnp.int32, gates.shape, 1)
    beta = jnp.sum(jnp.where(lane == head, gates, 0.0), axis=1, keepdims=True)
    g = jnp.sum(jnp.where(lane == head + N_HEADS_A, gates, 0.0), axis=1, keepdims=True)
    return beta, g


_B_NN, _B_NT, _B_TN = ((2,), (1,)), ((2,), (2,)), ((1,), (1,))


def _bdot(a, b, dims=_B_NN, prec=None):
    if prec is None:
        a, b = a.astype(BF16), b.astype(BF16)
    return lax.dot_general(a, b, (dims, ((0,), (0,))), precision=prec, preferred_element_type=F32)


def _heads_of(ref):
    return jnp.stack([ref[:, h * HEAD_A:(h + 1) * HEAD_A] for h in range(N_HEADS_A)])


def _all_head_gates(gates):
    pairs = [_head_gates(gates, h) for h in range(N_HEADS_A)]
    return jnp.stack([b for b, _ in pairs]), jnp.stack([g for _, g in pairs])


def _gdr_terms(k, beta, g):
    h, c = k.shape[0], GDR_CHUNK
    row = lax.broadcasted_iota(jnp.int32, (c, c), 0)
    col = lax.broadcasted_iota(jnp.int32, (c, c), 1)
    causal, strict = row >= col, row > col
    lower = jnp.broadcast_to(causal.astype(F32), (h, c, c))
    gcum = _bdot(lower, jnp.broadcast_to(g, (h, c, c)), prec=HIGHEST)
    diff = gcum - jnp.swapaxes(gcum, 1, 2)
    decay = jnp.where(causal, jnp.exp(jnp.where(causal, diff, 0.0)), 0.0)
    kb = k * beta
    return row, col, causal, strict, gcum, decay, kb, _bdot(kb, k, _B_NT)


def _unit_lower_inverses(a):
    c = a.shape[1]
    eye = (lax.broadcasted_iota(jnp.int32, (c, c), 0) == lax.broadcasted_iota(jnp.int32, (c, c), 1)).astype(F32)
    p = -a
    inv = eye + p
    step = 1
    while 2 * step < c:
        p = _bdot(p, p, prec=HIGH)
        inv = inv + _bdot(inv, p, prec=HIGH)
        step *= 2
    return inv


def _gdr_fwd(q, k, v, gates, name, comm=None):
    t = q.shape[0]
    c, nh = GDR_CHUNK, N_HEADS_A
    n = t // c

    def body(q_ref, k_ref, v_ref, gb_ref, o_ref, tm_ref, s_ref, state):
        @pl.when(pl.program_id(0) == 0)
        def _():
            state[...] = jnp.zeros_like(state)

        qv, kv, vv = _heads_of(q_ref), _heads_of(k_ref), _heads_of(v_ref)
        beta, g = _all_head_gates(gb_ref[...])
        row, col, causal, strict, gcum, decay, kb, kk = _gdr_terms(kv, beta, g)
        tm = _unit_lower_inverses(jnp.where(strict, kk * decay, 0.0))
        e = jnp.exp(gcum)
        u = _bdot(tm, vv * beta, prec=HIGH)
        w = _bdot(tm, kb * e, prec=HIGH)
        p = jnp.where(causal, _bdot(qv, kv, _B_NT) * decay, 0.0)
        s = state[...]
        s_ref[:, 0] = s
        tm_ref[:, 0] = tm
        vn = u - _bdot(w, s)
        o = _bdot(qv * e, s) + _bdot(p, vn)
        for h in range(nh):
            o_ref[:, h * HEAD_A:(h + 1) * HEAD_A] = o[h]
        glast = gcum[:, c - 1:c, :]
        state[...] = s * jnp.exp(glast) + _bdot(kv * jnp.exp(glast - gcum), vn, _B_TN)

    blk = pl.BlockSpec((c, WIDTH_A), lambda i: (i, 0))
    mat = pl.BlockSpec((nh, 1, c, c), lambda i: (0, i, 0, 0))
    return _call(
        body, name=name, grid=(n,), in_specs=[blk, blk, blk, pl.BlockSpec((c, LANE), lambda i: (i, 0))],
        out_specs=[blk, mat, mat],
        out_shape=[jax.ShapeDtypeStruct((t, WIDTH_A), F32), jax.ShapeDtypeStruct((nh, n, c, c), F32),
                   jax.ShapeDtypeStruct((nh, n, HEAD_A, HEAD_A), F32)],
        scratch_shapes=[pltpu.VMEM((nh, HEAD_A, HEAD_A), F32)], sem=("arbitrary",),
        args=(q, k, v, gates), comm=comm)


def _gdr_bwd(q, k, v, gates, tm_all, s_all, do, name, comm=None):
    t = q.shape[0]
    c, nh = GDR_CHUNK, N_HEADS_A
    n = t // c

    def body(q_ref, k_ref, v_ref, gb_ref, tm_ref, s_ref, do_ref, dq_ref, dk_ref, dv_ref, dgb_ref, dstate):
        @pl.when(pl.program_id(0) == 0)
        def _():
            dstate[...] = jnp.zeros_like(dstate)

        qv, kv, vv, dov = _heads_of(q_ref), _heads_of(k_ref), _heads_of(v_ref), _heads_of(do_ref)
        beta, g = _all_head_gates(gb_ref[...])
        tm, s, dsp = tm_ref[:, 0], s_ref[:, 0], dstate[...]
        row, col, causal, strict, gcum, decay, kb, kk = _gdr_terms(kv, beta, g)
        rowsum = lambda x: jnp.sum(x, axis=2, keepdims=True)
        e = jnp.exp(gcum)
        vb, kbe = vv * beta, kb * e
        u = _bdot(tm, vb, prec=HIGH)
        w = _bdot(tm, kbe, prec=HIGH)
        qk = _bdot(qv, kv, _B_NT)
        p = jnp.where(causal, qk * decay, 0.0)
        vn = u - _bdot(w, s)
        glast = gcum[:, c - 1:c, :]
        el = jnp.exp(glast)
        f = jnp.exp(glast - gcum)
        kd = kv * f
        qe = qv * e

        dvn = _bdot(p, dov, _B_TN) + _bdot(kd, dsp)
        dglast = el[:, :, 0:1] * jnp.sum(s * dsp, axis=(1, 2), keepdims=True)
        dkd = _bdot(vn, dsp, _B_NT)
        dk = dkd * f
        df = rowsum(dkd * kv) * f[:, :, 0:1]
        dglast = dglast + jnp.sum(df, axis=1, keepdims=True)
        dgc = -df
        dp = jnp.where(causal, _bdot(dov, vn, _B_NT), 0.0)
        dqe = _bdot(dov, s, _B_NT)
        dq = dqe * e
        de = rowsum(dqe * qv)
        dstate[...] = dsp * el + _bdot(qe, dov, _B_TN) - _bdot(w, dvn, _B_TN)
        dw = -_bdot(dvn, s, _B_NT)
        dvb = _bdot(tm, dvn, _B_TN, prec=HIGH)
        dkbe = _bdot(tm, dw, _B_TN, prec=HIGH)
        da = -jnp.where(strict, _bdot(dvb, u, _B_NT) + _bdot(dkbe, w, _B_NT), 0.0)
        dkk = da * decay
        dqk = dp * decay
        dd = da * kk + dp * qk
        dq = dq + _bdot(dqk, kv)
        dk = dk + _bdot(dqk, qv, _B_TN)
        dkb = _bdot(dkk, kv) + dkbe * e
        dk = dk + _bdot(dkk, kb, _B_TN)
        de = de + rowsum(dkbe * kb)
        dk = dk + dkb * beta
        dbeta = rowsum(dkb * kv) + rowsum(dvb * vv)
        m = dd * decay
        dgc = dgc + rowsum(m) - rowsum(jnp.swapaxes(m, 1, 2))
        dgc = dgc + de * e[:, :, 0:1]
        dgc = dgc + jnp.where(row[:, 0:1] == c - 1, dglast, 0.0)
        upper = jnp.broadcast_to((row <= col).astype(F32), (nh, c, c))
        dg = _bdot(upper, jnp.broadcast_to(dgc, (nh, c, c)), prec=HIGHEST)
        dv = dvb * beta
        for h in range(nh):
            cols = slice(h * HEAD_A, (h + 1) * HEAD_A)
            dq_ref[:, cols] = dq[h]
            dk_ref[:, cols] = dk[h]
            dv_ref[:, cols] = dv[h]
        head = lax.broadcasted_iota(jnp.int32, (nh, c, LANE), 0)
        lane = lax.broadcasted_iota(jnp.int32, (nh, c, LANE), 2)
        dgb_ref[...] = jnp.where(lane == head, dbeta, jnp.where(lane == head + nh, dg, 0.0))

    blk = pl.BlockSpec((c, WIDTH_A), lambda i: (n - 1 - i, 0))
    mat = pl.BlockSpec((nh, 1, c, c), lambda i: (0, n - 1 - i, 0, 0))
    return _call(
        body, name=name, grid=(n,),
        in_specs=[blk, blk, blk, pl.BlockSpec((c, LANE), lambda i: (n - 1 - i, 0)), mat, mat, blk],
        out_specs=[blk, blk, blk, pl.BlockSpec((nh, c, LANE), lambda i: (0, n - 1 - i, 0))],
        out_shape=[jax.ShapeDtypeStruct((t, WIDTH_A), F32)] * 3 + [jax.ShapeDtypeStruct((nh, t, LANE), F32)],
        scratch_shapes=[pltpu.VMEM((nh, HEAD_A, HEAD_A), F32)], sem=("arbitrary",),
        args=(q, k, v, gates, tm_all, s_all, do), comm=comm)


def _onorm_fwd(o, gate, g, name):
    t = o.shape[0]

    def body(o_ref, gate_ref, g_ref, y_ref):
        ov, gv = o_ref[...], gate_ref[...]
        r = lax.rsqrt(jnp.mean(ov * ov, axis=-1, keepdims=True) + RMS_EPS)
        y_ref[...] = (ov * r * g_ref[...] * gv * _sigmoid(gv)).astype(BF16)

    blk = pl.BlockSpec((t, HEAD_A), lambda j: (0, j))
    return pl.pallas_call(
        body, name=name, grid=(N_HEADS_A,), in_specs=[blk, blk, pl.BlockSpec((1, HEAD_A), lambda j: (0, 0))],
        out_specs=blk, out_shape=jax.ShapeDtypeStruct((t, WIDTH_A), BF16),
        compiler_params=_params(("parallel",)))(o, gate, g)


def _onorm_bwd(o, gate, g, dy, name):
    t = o.shape[0]

    def body(o_ref, gate_ref, g_ref, dy_ref, do_ref, dgate_ref, dg_ref):
        @pl.when(pl.program_id(0) == 0)
        def _():
            dg_ref[...] = jnp.zeros_like(dg_ref)

        ov, gv, dyv = o_ref[...], gate_ref[...], dy_ref[...].astype(F32)
        r = lax.rsqrt(jnp.mean(ov * ov, axis=-1, keepdims=True) + RMS_EPS)
        oh = ov * r
        sg, dsg = _silu_and_grad(gv)
        dgate_ref[...] = (dyv * oh * g_ref[...] * dsg).astype(BF16)
        dn = dyv * sg
        dg_ref[...] += jnp.sum(dn * oh, axis=0, keepdims=True)
        dng = dn * g_ref[...]
        do_ref[...] = r * (dng - oh * jnp.mean(dng * oh, axis=-1, keepdims=True))

    blk = pl.BlockSpec((t, HEAD_A), lambda j: (0, j))
    vec = pl.BlockSpec((1, HEAD_A), lambda j: (0, 0))
    return pl.pallas_call(
        body, name=name, grid=(N_HEADS_A,), in_specs=[blk, blk, vec, blk], out_specs=[blk, blk, vec],
        out_shape=[jax.ShapeDtypeStruct((t, WIDTH_A), F32), jax.ShapeDtypeStruct((t, WIDTH_A), BF16),
                   jax.ShapeDtypeStruct((1, HEAD_A), F32)],
        compiler_params=_params(("arbitrary",)))(o, gate, g, dy)


def _cmul(ar, ai, br, bi):
    return ar * br - ai * bi, ar * bi + ai * br


def _scan_tables(ar, ai, reverse):
    p1 = (ar, ai)
    p2 = _cmul(*p1, *p1)
    p4 = _cmul(*p2, *p2)
    p8 = _cmul(*p4, *p4)
    p3 = _cmul(*p2, *p1)
    p5 = _cmul(*p4, *p1)
    p6 = _cmul(*p4, *p2)
    p7 = _cmul(*p4, *p3)
    pows = [p1, p2, p3, p4, p5, p6, p7, p8]
    rows = lax.broadcasted_iota(jnp.int32, (8, ar.shape[1]), 0)
    tr = jnp.zeros((8, ar.shape[1]), F32)
    ti = jnp.zeros((8, ar.shape[1]), F32)
    for r in range(8):
        pw = pows[7 - r] if reverse else pows[r]
        tr = jnp.where(rows == r, pw[0], tr)
        ti = jnp.where(rows == r, pw[1], ti)
    return p1, p2, p4, p8, tr, ti


def _tile_scan(xr, xi, p1, p2, p4, reverse):
    rows = lax.broadcasted_iota(jnp.int32, xr.shape, 0)
    for s, (pr, pi) in ((1, p1), (2, p2), (4, p4)):
        if reverse:
            keep = rows < 8 - s
            sr, si = pltpu.roll(xr, 8 - s, 0), pltpu.roll(xi, 8 - s, 0)
        else:
            keep = rows >= s
            sr, si = pltpu.roll(xr, s, 0), pltpu.roll(xi, s, 0)
        sr, si = jnp.where(keep, sr, 0.0), jnp.where(keep, si, 0.0)
        mr, mi = _cmul(pr, pi, sr, si)
        xr, xi = xr + mr, xi + mi
    return xr, xi


def _s5_scan_fwd(bu, a, name, tb=512, comm=None):
    t = bu.shape[0]
    cb = SCAN_CB
    nt = t // tb

    def body(b_ref, a_ref, x_ref, carry):
        @pl.when(pl.program_id(1) == 0)
        def _():
            carry[...] = jnp.zeros_like(carry)

        ar, ai = a_ref[:, 0:cb], a_ref[:, cb:2 * cb]
        p1, p2, p4, p8, tr, ti = _scan_tables(ar, ai, False)

        def step(j, c):
            cr, ci = c
            i = pl.multiple_of(j * 8, 8)
            xr, xi = _tile_scan(b_ref[pl.ds(i, 8), 0:cb], b_ref[pl.ds(i, 8), cb:2 * cb], p1, p2, p4, False)
            mr, mi = _cmul(tr, ti, cr, ci)
            xr, xi = xr + mr, xi + mi
            x_ref[pl.ds(i, 8), 0:cb] = xr
            x_ref[pl.ds(i, 8), cb:2 * cb] = xi
            return xr[7:8, :], xi[7:8, :]

        cr, ci = lax.fori_loop(0, tb // 8, step, (carry[0:1, :], carry[1:2, :]), unroll=SCAN_UNROLL)
        carry[0:1, :] = cr
        carry[1:2, :] = ci

    blk = pl.BlockSpec((tb, 2 * cb), lambda j, i: (i, j))
    return _call(
        body, name=name, grid=(SSM_CH // cb, nt),
        in_specs=[blk, pl.BlockSpec((1, 2 * cb), lambda j, i: (0, j))], out_specs=blk,
        out_shape=jax.ShapeDtypeStruct((t, 2 * SSM_CH), F32), scratch_shapes=[pltpu.VMEM((8, cb), F32)],
        sem=("parallel", "arbitrary"), args=(bu, a), comm=comm)


def _s5_scan_bwd(dx, x, a, name, tb=512, comm=None):
    t = dx.shape[0]
    cb = SCAN_CB
    nt = t // tb
    nj = tb // 8

    def body(d_ref, x_ref, xp_ref, a_ref, l_ref, da_ref, carry, acc):
        tblk = pl.program_id(1)

        @pl.when(tblk == 0)
        def _():
            carry[...] = jnp.zeros_like(carry)
            acc[...] = jnp.zeros_like(acc)

        ar, ai = a_ref[:, 0:cb], a_ref[:, cb:2 * cb]
        p1, p2, p4, p8, tr, ti = _scan_tables(ar, -ai, True)
        rows = lax.broadcasted_iota(jnp.int32, (8, cb), 0)

        def step(jj, c):
            cr, ci, sr_acc, si_acc = c
            j = nj - 1 - jj
            i = pl.multiple_of(j * 8, 8)
            lr, li = _tile_scan(d_ref[pl.ds(i, 8), 0:cb], d_ref[pl.ds(i, 8), cb:2 * cb], p1, p2, p4, True)
            mr, mi = _cmul(tr, ti, cr, ci)
            lr, li = lr + mr, li + mi
            l_ref[pl.ds(i, 8), 0:cb] = lr
            l_ref[pl.ds(i, 8), cb:2 * cb] = li
            ip = pl.multiple_of(jnp.maximum(j - 1, 0) * 8, 8)
            prev_r = jnp.where(j > 0, x_ref[pl.ds(ip, 8), 0:cb], xp_ref[:, 0:cb])
            prev_i = jnp.where(j > 0, x_ref[pl.ds(ip, 8), cb:2 * cb], xp_ref[:, cb:2 * cb])
            edge = jnp.where(jnp.logical_and(j == 0, tblk == nt - 1), 0.0, 1.0)
            xs_r = jnp.where(rows == 0, pltpu.roll(prev_r, 1, 0) * edge, pltpu.roll(x_ref[pl.ds(i, 8), 0:cb], 1, 0))
            xs_i = jnp.where(rows == 0, pltpu.roll(prev_i, 1, 0) * edge, pltpu.roll(x_ref[pl.ds(i, 8), cb:2 * cb], 1, 0))
            sr_acc = sr_acc + lr * xs_r + li * xs_i
            si_acc = si_acc + li * xs_r - lr * xs_i
            return lr[0:1, :], li[0:1, :], sr_acc, si_acc

        cr, ci, sr_acc, si_acc = lax.fori_loop(
            0, nj, step, (carry[0:1, :], carry[1:2, :], acc[:, 0:cb], acc[:, cb:2 * cb]), unroll=SCAN_UNROLL)
        carry[0:1, :] = cr
        carry[1:2, :] = ci
        acc[:, 0:cb] = sr_acc
        acc[:, cb:2 * cb] = si_acc

        @pl.when(tblk == nt - 1)
        def _():
            da_ref[...] = jnp.sum(acc[...], axis=0, keepdims=True)

    blk = pl.BlockSpec((tb, 2 * cb), lambda j, i: (nt - 1 - i, j))
    prev = pl.BlockSpec((8, 2 * cb), lambda j, i: (jnp.maximum((nt - 1 - i) * (tb // 8) - 1, 0), j))
    vec = pl.BlockSpec((1, 2 * cb), lambda j, i: (0, j))
    return _call(
        body, name=name, grid=(SSM_CH // cb, nt), in_specs=[blk, blk, prev, vec], out_specs=[blk, vec],
        out_shape=[jax.ShapeDtypeStruct((t, 2 * SSM_CH), F32), jax.ShapeDtypeStruct((1, 2 * SSM_CH), F32)],
        scratch_shapes=[pltpu.VMEM((8, cb), F32), pltpu.VMEM((8, 2 * cb), F32)],
        sem=("parallel", "arbitrary"), args=(dx, x, x, a), comm=comm)


def _glu_fwd(yc, u, dvec, wg, bg, name, tr=256):
    t = yc.shape[0]

    def body(yc_ref, u_ref, d_ref, w_ref, b_ref, yl_ref, yb_ref):
        yl = yc_ref[...] + d_ref[...] * u_ref[...]
        yl_ref[...] = yl
        yg, _ = _gelu_and_grad(yl)
        z = jnp.dot(yg.astype(BF16), w_ref[...], preferred_element_type=F32) + b_ref[...]
        yb_ref[...] = (yg * _sigmoid(z)).astype(BF16)

    blk = pl.BlockSpec((tr, SSM_WIDTH), lambda i: (i, 0))
    vec = pl.BlockSpec((1, SSM_WIDTH), lambda i: (0, 0))
    return pl.pallas_call(
        body, name=name, grid=(t // tr,),
        in_specs=[blk, blk, vec, pl.BlockSpec((SSM_WIDTH, SSM_WIDTH), lambda i: (0, 0)), vec],
        out_specs=[blk, blk],
        out_shape=[jax.ShapeDtypeStruct((t, SSM_WIDTH), F32), jax.ShapeDtypeStruct((t, SSM_WIDTH), BF16)],
        compiler_params=_params(("parallel",)))(yc, u, dvec, wg, bg)


def _glu_bwd(yl, u, dvec, wg, bg, dyb, name, tr=256):
    t = yl.shape[0]

    def body(yl_ref, u_ref, d_ref, w_ref, b_ref, dy_ref, dyl_ref, du_ref, dw_ref, db_ref, dd_ref):
        @pl.when(pl.program_id(0) == 0)
        def _():
            dw_ref[...] = jnp.zeros_like(dw_ref)
            db_ref[...] = jnp.zeros_like(db_ref)
            dd_ref[...] = jnp.zeros_like(dd_ref)

        ylv, dyv, wv = yl_ref[...], dy_ref[...].astype(F32), w_ref[...]
        yg, dgelu = _gelu_and_grad(ylv)
        ygb = yg.astype(BF16)
        z = jnp.dot(ygb, wv, preferred_element_type=F32) + b_ref[...]
        sg = _sigmoid(z)
        dz = dyv * yg * sg * (1.0 - sg)
        dzb = dz.astype(BF16)
        dyg = dyv * sg + lax.dot_general(dzb, wv, (((1,), (1,)), ((), ())), preferred_element_type=F32)
        dyl = dyg * dgelu
        dyl_ref[...] = dyl.astype(BF16)
        du_ref[...] = dyl * d_ref[...]
        dw_ref[...] += lax.dot_general(ygb, dzb, (((0,), (0,)), ((), ())), preferred_element_type=F32)
        db_ref[...] += jnp.sum(dz, axis=0, keepdims=True)
        dd_ref[...] += jnp.sum(dyl * u_ref[...], axis=0, keepdims=True)

    blk = pl.BlockSpec((tr, SSM_WIDTH), lambda i: (i, 0))
    vec = pl.BlockSpec((1, SSM_WIDTH), lambda i: (0, 0))
    wsp = pl.BlockSpec((SSM_WIDTH, SSM_WIDTH), lambda i: (0, 0))
    return pl.pallas_call(
        body, name=name, grid=(t // tr,), in_specs=[blk, blk, vec, wsp, vec, blk],
        out_specs=[blk, blk, wsp, vec, vec],
        out_shape=[jax.ShapeDtypeStruct((t, SSM_WIDTH), BF16), jax.ShapeDtypeStruct((t, SSM_WIDTH), F32),
                   jax.ShapeDtypeStruct((SSM_WIDTH, SSM_WIDTH), F32), jax.ShapeDtypeStruct((1, SSM_WIDTH), F32),
                   jax.ShapeDtypeStruct((1, SSM_WIDTH), F32)],
        compiler_params=_params(("arbitrary",)))(yl, u, dvec, wg, bg, dyb)


def _mesh_pos():
    return lax.axis_index("x"), lax.axis_index("y"), lax.axis_index("c")


def _device_index():
    x, y, c = _mesh_pos()
    return 4 * x + 2 * y + c


def _gather_comm(arrays):
    na = len(arrays)

    def own_copy(ins, outs, sems, ai):
        return pltpu.make_async_copy(ins[ai], outs[ai].at[_device_index()], sems[2].at[ai])

    def ctx(ins, outs, sems):
        send_sems, recv_sems = sems[:2]
        x, y, c = _mesh_pos()
        chips = [(1 - x, y), (x, 1 - y), (1 - x, 1 - y)]

        def copy(ai, kk, block, to, own=False):
            slot = outs[ai].at[4 * block[0] + 2 * block[1] + block[2]]
            return pltpu.make_async_remote_copy(
                src_ref=ins[ai] if own else slot, dst_ref=slot, send_sem=send_sems.at[ai, kk],
                recv_sem=recv_sems.at[ai, kk], device_id=to, device_id_type=MESH)

        return (x, y, c), (x, y, 1 - c), chips, c, copy

    def start(ins, outs, sems):
        me, sibling, chips, c, copy = ctx(ins, outs, sems)
        for ai in range(na):
            copy(ai, 0, me, sibling, own=True).start()
            for j, chip in enumerate(chips):
                copy(ai, 1 + j, me, (*chip, c), own=True).start()
        for ai in range(na):
            own_copy(ins, outs, sems, ai).start()

    def mid(ins, outs, sems):
        me, sibling, chips, c, copy = ctx(ins, outs, sems)
        for ai in range(na):
            for j, chip in enumerate(chips):
                copy(ai, 1 + j, (*chip, c), me).wait_recv()
                copy(ai, 4 + j, (*chip, c), sibling).start()

    def end(ins, outs, sems):
        me, sibling, chips, c, copy = ctx(ins, outs, sems)
        for ai in range(na):
            copy(ai, 0, sibling, me).wait_recv()
            copy(ai, 0, me, sibling, own=True).wait_send()
            for j, chip in enumerate(chips):
                copy(ai, 4 + j, (*chip, 1 - c), me).wait_recv()
                copy(ai, 1 + j, me, (*chip, c), own=True).wait_send()
                copy(ai, 4 + j, (*chip, c), sibling).wait_send()
            own_copy(ins, outs, sems, ai).wait()

    return Comm(arrays, [jax.ShapeDtypeStruct((N_DEV,) + a.shape, a.dtype) for a in arrays],
                [pltpu.SemaphoreType.DMA((na, 7)), pltpu.SemaphoreType.DMA((na, 7)), pltpu.SemaphoreType.DMA((na,))],
                start, end, mid)


def _sequencer_gather(arrays, name, collective_id):
    comm = _gather_comm(arrays)
    na = len(arrays)

    def body(*refs):
        ins, outs, sems = refs[:na], refs[na:2 * na], refs[2 * na:]
        x, y, c = _mesh_pos()
        peers = [(x, y, 1 - c), (1 - x, y, c), (x, 1 - y, c), (1 - x, 1 - y, c)]
        barrier = pltpu.get_barrier_semaphore()
        for peer in peers:
            pl.semaphore_signal(barrier, inc=1, device_id=peer, device_id_type=MESH)
        pl.semaphore_wait(barrier, len(peers))
        comm.start(ins, outs, sems)
        comm.mid(ins, outs, sems)
        comm.end(ins, outs, sems)

    return list(pl.kernel(
        body, out_type=tuple(comm.out_shapes), mesh=plsc.ScalarSubcoreMesh(axis_name="sequencer", num_cores=1),
        name=name, scratch_types=tuple(comm.sems),
        compiler_params=pltpu.CompilerParams(collective_id=collective_id))(*arrays))


def _sequencer_exchange(comm, peers_of, name, collective_id):
    na = len(comm.inputs)

    def body(*refs):
        ins, outs, sems = refs[:na], refs[na:na + len(comm.out_shapes)], refs[na + len(comm.out_shapes):]
        peers = peers_of(*_mesh_pos())
        barrier = pltpu.get_barrier_semaphore()
        for peer in peers:
            pl.semaphore_signal(barrier, inc=1, device_id=peer, device_id_type=MESH)
        pl.semaphore_wait(barrier, len(peers))
        comm.start(ins, outs, sems)
        comm.end(ins, outs, sems)

    return list(pl.kernel(
        body, out_type=tuple(comm.out_shapes), mesh=plsc.ScalarSubcoreMesh(axis_name="sequencer", num_cores=1),
        name=name, scratch_types=tuple(comm.sems),
        compiler_params=pltpu.CompilerParams(collective_id=collective_id))(*comm.inputs))


SIBLING_SWAP_ID, CHIP_EXCHANGE_ID = 9, 10


def _sequencer_swap(arrays, name):
    return _sequencer_exchange(_swap_comm(arrays), lambda x, y, c: [(x, y, 1 - c)], name, SIBLING_SWAP_ID)[0]


def _sequencer_chips(send, name):
    return _sequencer_exchange(_chips_comm(send), lambda x, y, c: [(1 - x, y, c), (x, 1 - y, c), (1 - x, 1 - y, c)],
                               name, CHIP_EXCHANGE_ID)[0]


def _swap_comm(arrays):
    na = len(arrays)
    offs = np.concatenate([[0], np.cumsum([a.shape[1] for a in arrays])]).astype(int)

    def copies(ins, outs, sems):
        x, y, c = _mesh_pos()
        return [pltpu.make_async_remote_copy(
            src_ref=ins[ai].at[2 * k + 1 - c], dst_ref=outs[0].at[k, pl.ds(int(offs[ai]), arrays[ai].shape[1])],
            send_sem=sems[0].at[ai, k], recv_sem=sems[1].at[ai, k], device_id=(x, y, 1 - c), device_id_type=MESH)
            for ai in range(na) for k in range(4)]

    def start(ins, outs, sems):
        for cp in copies(ins, outs, sems):
            cp.start()

    def end(ins, outs, sems):
        for cp in copies(ins, outs, sems):
            cp.wait()

    return Comm(arrays, [jax.ShapeDtypeStruct((4, int(offs[-1]), PACK_COLS), arrays[0].dtype)],
                [pltpu.SemaphoreType.DMA((na, 4)), pltpu.SemaphoreType.DMA((na, 4))], start, end)


def _chips_comm(send):
    def copies(ins, outs, sems):
        x, y, c = _mesh_pos()
        chips = [(1 - x, y), (x, 1 - y), (1 - x, 1 - y)]
        return [pltpu.make_async_remote_copy(
            src_ref=ins[0].at[2 * cx + cy], dst_ref=outs[0].at[j], send_sem=sems[0].at[j], recv_sem=sems[1].at[j],
            device_id=(cx, cy, c), device_id_type=MESH) for j, (cx, cy) in enumerate(chips)]

    def start(ins, outs, sems):
        for cp in copies(ins, outs, sems):
            cp.start()

    def end(ins, outs, sems):
        for cp in copies(ins, outs, sems):
            cp.wait()

    return Comm([send], [jax.ShapeDtypeStruct((3,) + send.shape[1:], send.dtype)],
                [pltpu.SemaphoreType.DMA((3,)), pltpu.SemaphoreType.DMA((3,))], start, end)


def _pair_sum(keep, recv, name, tr=464):
    nchip, rows, cols = keep.shape

    def body(g_ref, r_ref, o_ref):
        o_ref[...] = (g_ref[...].astype(F32) + r_ref[...].astype(F32)).astype(BF16)

    blk = pl.BlockSpec((1, tr, cols), lambda k, i: (k, i, 0))
    return pl.pallas_call(
        body, name=name, grid=(nchip, rows // tr), in_specs=[blk, blk], out_specs=blk,
        out_shape=jax.ShapeDtypeStruct((nchip, rows, cols), BF16),
        compiler_params=_params(("parallel", "parallel")))(keep, recv)


def _pair_sum_pieces(pieces, recv, name, tr):
    _, rows, cols = pieces.shape
    core = lax.axis_index("c").astype(jnp.int32).reshape(1)

    def body(c_ref, g_ref, r_ref, o_ref):
        del c_ref
        o_ref[...] = (g_ref[...].astype(F32) + r_ref[...].astype(F32)).astype(BF16)

    grid_spec = pltpu.PrefetchScalarGridSpec(
        num_scalar_prefetch=1, grid=(4, rows // tr),
        in_specs=[pl.BlockSpec((1, tr, cols), lambda k, i, c_ref: (2 * k + c_ref[0], i, 0)),
                  pl.BlockSpec((1, tr, cols), lambda k, i, c_ref: (k, i, 0))],
        out_specs=pl.BlockSpec((1, tr, cols), lambda k, i, c_ref: (k, i, 0)))
    return pl.pallas_call(
        body, name=name, grid_spec=grid_spec, out_shape=jax.ShapeDtypeStruct((4, rows, cols), BF16),
        compiler_params=_params(("parallel", "parallel")))(core, pieces, recv)


def _chip_sum(own, others, name, tr=464):
    _, rows, cols = own.shape
    chip = (2 * lax.axis_index("x") + lax.axis_index("y")).astype(jnp.int32).reshape(1)

    def body(chip_ref, own_ref, oth_ref, o_ref):
        del chip_ref
        acc = own_ref[0].astype(F32)
        for j in range(3):
            acc = acc + oth_ref[j].astype(F32)
        o_ref[...] = acc

    grid_spec = pltpu.PrefetchScalarGridSpec(
        num_scalar_prefetch=1, grid=(rows // tr,),
        in_specs=[pl.BlockSpec((1, tr, cols), lambda i, chip_ref: (chip_ref[0], i, 0)),
                  pl.BlockSpec((3, tr, cols), lambda i, chip_ref: (0, i, 0))],
        out_specs=pl.BlockSpec((tr, cols), lambda i, chip_ref: (i, 0)))
    return pl.pallas_call(
        body, name=name, grid_spec=grid_spec, out_shape=jax.ShapeDtypeStruct((rows, cols), F32),
        compiler_params=_params(("parallel",)))(chip, own, others)


def _sum_leading(parts, name, tr=464):
    nparts, rows, cols = parts.shape
    tr = tr if rows % tr == 0 else rows

    def body(p_ref, o_ref):
        acc = p_ref[0].astype(F32)
        for i in range(1, nparts):
            acc = acc + p_ref[i].astype(F32)
        o_ref[...] = acc

    return pl.pallas_call(
        body, name=name, grid=(rows // tr,),
        in_specs=[pl.BlockSpec((nparts, tr, cols), lambda i: (0, i, 0))],
        out_specs=pl.BlockSpec((tr, cols), lambda i: (i, 0)), out_shape=jax.ShapeDtypeStruct((rows, cols), F32),
        compiler_params=_params(("parallel",)))(parts)


def _adamw(w, g, m, v, name, comm=None):
    shape = w.shape
    cols = shape[-1]
    lead = shape[0] if len(shape) >= 3 else 1
    rows = int(np.prod(shape[:-1])) // lead if len(shape) > 1 else 1
    w2, g2, m2, v2 = (a.reshape(lead, rows, cols) for a in (w, g, m, v))
    tr = rows
    for cand in (512, 256, 128, 64, 32, 16, 8):
        if rows % cand == 0 and rows > cand:
            tr = cand
            break
    bc1, bc2 = 1.0 - ADAM_B1 ** ADAM_STEP, 1.0 - ADAM_B2 ** ADAM_STEP

    def body(w_ref, g_ref, m_ref, v_ref, d_ref, nm_ref, nv_ref):
        gv = g_ref[...]
        nm = ADAM_B1 * m_ref[...] + (1.0 - ADAM_B1) * gv
        nv = ADAM_B2 * v_ref[...] + (1.0 - ADAM_B2) * (gv * gv)
        nm_ref[...] = nm
        nv_ref[...] = nv
        d_ref[...] = -ADAM_LR * ((nm / bc1) / (jnp.sqrt(nv / bc2) + ADAM_EPS) + ADAM_WD * w_ref[...])

    blk = pl.BlockSpec((1, tr, cols), lambda l, i: (l, i, 0))
    res = _call(body, name=name, grid=(lead, rows // tr), in_specs=[blk] * 4, out_specs=[blk] * 3,
                out_shape=[jax.ShapeDtypeStruct((lead, rows, cols), F32)] * 3, sem=("parallel", "parallel"),
                args=(w2, g2, m2, v2), comm=comm)
    outs, couts = res if comm is not None else (res, None)
    outs = tuple(o.reshape(shape) for o in outs)
    return outs if comm is None else (outs, couts)


WEIGHT_NAMES = ['norm_mix_g', 'norm_xa_g', 'norm_ffn_g', 'norm_mem_g', 'norm_final_g', 'w_in_ab', 'conv_qkv_a',
                'a_log_a', 'dt_bias_a', 'onorm_g_a', 'ssm_lambda_re', 'ssm_lambda_im', 'ssm_b_re', 'ssm_b_im',
                'ssm_c_re', 'ssm_c_im', 'ssm_d', 'ssm_log_dt', 'w_glu_b', 'b_glu_b', 'w_out_ab', 'pool_w',
                'pool_scale', 'xa_wq', 'xa_wkv', 'xa_wo', 'ffn_w_up', 'ffn_conv', 'ffn_w_down']
BIG_SHARDED = {'w_in_ab': ((1, 1024, 2568), 2), 'w_glu_b': ((1, 512, 512), 1), 'w_out_ab': ((1, 1024, 1024), 1),
               'pool_w': ((1, 4, 256, 256), 2), 'xa_wq': ((2, 1024, 1024), 1), 'xa_wkv': ((2, 1024, 2048), 2),
               'xa_wo': ((2, 1024, 1024), 1), 'ffn_w_up': ((2, 1024, 5632), 2), 'ffn_w_down': ((2, 2816, 1024), 1)}
SMALL_SHARDED = {'conv_qkv_a': ((1, 4, 1536), 2), 'pool_scale': ((1, 1024), 1), 'ffn_conv': ((2, 3, 5632), 2)}
REPLICATED = {'norm_mix_g': (2, 1024), 'norm_xa_g': (2, 1024), 'norm_ffn_g': (2, 1024), 'norm_mem_g': (1024,),
              'norm_final_g': (1024,), 'a_log_a': (1, 4), 'dt_bias_a': (1, 4), 'onorm_g_a': (1, 128),
              'ssm_lambda_re': (1, 32, 64), 'ssm_lambda_im': (1, 32, 64), 'ssm_b_re': (1, 32, 64, 16),
              'ssm_b_im': (1, 32, 64, 16), 'ssm_c_re': (1, 32, 16, 64), 'ssm_c_im': (1, 32, 16, 64),
              'ssm_d': (1, 32, 16), 'ssm_log_dt': (1, 32), 'b_glu_b': (1, 512)}
PACK_ROW_ALIGN = 8


def _shard_shape(shape, axis):
    return tuple(s // N_DEV if i == axis else s for i, s in enumerate(shape))


def _round_up(n, m):
    return (n + m - 1) // m * m


def _pack(arrays):
    total = sum(int(np.prod(a.shape)) for a in arrays)
    padded = _round_up(total, PACK_COLS * PACK_ROW_ALIGN)
    parts = [a.astype(F32).reshape(-1) for a in arrays]
    if padded != total:
        parts.append(jnp.zeros((padded - total,), F32))
    return jnp.concatenate(parts).reshape(padded // PACK_COLS, PACK_COLS)


def _unpack(packed, shapes):
    flat, out, off = packed.reshape(-1), [], 0
    for shape in shapes:
        size = int(np.prod(shape))
        out.append(flat[off:off + size].reshape(shape))
        off += size
    return out


def _split_shards(full, axis):
    shape = full.shape
    s = shape[axis] // N_DEV
    a = full.reshape(shape[:axis] + (N_DEV, s) + shape[axis + 1:])
    return jnp.moveaxis(a, axis, 0).reshape(N_DEV, -1)


def _merge_shards(pieces, shape, axis):
    sh = _shard_shape(shape, axis)
    a = pieces.reshape((N_DEV,) + sh)
    a = jnp.moveaxis(a, 0, axis)
    return a.reshape(shape)


_SCAN_NB = SSM_CH // SCAN_CB


def _to_scan_layout(m, axis):
    shape = m.shape
    m = m.reshape(shape[:axis] + (2, _SCAN_NB, SCAN_CB) + shape[axis + 1:])
    return jnp.swapaxes(m, axis, axis + 1).reshape(shape)


def _from_scan_layout(m, axis):
    shape = m.shape
    m = m.reshape(shape[:axis] + (_SCAN_NB, 2, SCAN_CB) + shape[axis + 1:])
    return jnp.swapaxes(m, axis, axis + 1).reshape(shape)


def _s5_discretise(lam_re, lam_im, b_re, b_im, log_dt):
    dt = jnp.exp(log_dt)[:, None]
    mag = jnp.exp(lam_re * dt)
    ang = lam_im * dt
    lb_re, lb_im = mag * jnp.cos(ang), mag * jnp.sin(ang)
    den = lam_re * lam_re + lam_im * lam_im
    nr, ni = lb_re - 1.0, lb_im
    coef_re = (nr * lam_re + ni * lam_im) / den
    coef_im = (ni * lam_re - nr * lam_im) / den
    bb_re = coef_re[..., None] * b_re - coef_im[..., None] * b_im
    bb_im = coef_re[..., None] * b_im + coef_im[..., None] * b_re
    return lb_re, lb_im, bb_re, bb_im


_GROUPS_PER_BLOCK = N_GROUPS // _SCAN_NB
_U_BLOCK = _GROUPS_PER_BLOCK * SSM_GROUP


def _s5_matrices(lb_re, lb_im, bb_re, bb_im, c_re, c_im):
    eye = jnp.eye(_GROUPS_PER_BLOCK, dtype=F32)
    blocked = lambda m: m.reshape((_SCAN_NB, _GROUPS_PER_BLOCK) + m.shape[1:])
    bmat = lambda bb: jnp.einsum('jgph,gk->jghkp', blocked(bb), eye).reshape(_SCAN_NB, _U_BLOCK, SCAN_CB)
    cmat = lambda cc: jnp.einsum('jghp,gk->jkpgh', blocked(cc), eye).reshape(_SCAN_NB, SCAN_CB, _U_BLOCK)
    b_in = jnp.concatenate([bmat(bb_re), bmat(bb_im)], axis=2)
    c_out = jnp.concatenate([cmat(c_re), -cmat(c_im)], axis=1)
    a_row = _to_scan_layout(jnp.concatenate([lb_re.reshape(1, SSM_CH), lb_im.reshape(1, SSM_CH)], axis=1), 1)
    return b_in, c_out, a_row


def _s5_matrix_grads(db_in, dc_out, da_row):
    da_nat = _from_scan_layout(da_row, 1)
    eye = jnp.eye(_GROUPS_PER_BLOCK, dtype=F32)
    nb, gb = _SCAN_NB, _GROUPS_PER_BLOCK
    bgrad = lambda m: jnp.einsum('jghkp,gk->jgph', m.reshape(nb, gb, SSM_GROUP, gb, SSM_STATE), eye
                                 ).reshape(N_GROUPS, SSM_STATE, SSM_GROUP)
    cgrad = lambda m: jnp.einsum('jkpgh,gk->jghp', m.reshape(nb, gb, SSM_STATE, gb, SSM_GROUP), eye
                                 ).reshape(N_GROUPS, SSM_GROUP, SSM_STATE)
    dbb_re, dbb_im = bgrad(db_in[:, :, :SCAN_CB]), bgrad(db_in[:, :, SCAN_CB:])
    dc_re, dc_im = cgrad(dc_out[:, :SCAN_CB]), -cgrad(dc_out[:, SCAN_CB:])
    dlb_re = da_nat[0, :SSM_CH].reshape(N_GROUPS, SSM_STATE)
    dlb_im = da_nat[0, SSM_CH:].reshape(N_GROUPS, SSM_STATE)
    return dlb_re, dlb_im, dbb_re, dbb_im, dc_re, dc_im


def _as_pieces(a):
    return a.reshape(N_DEV, a.shape[0] // N_DEV, a.shape[1])


def _hybrid_fwd(xn, x, wts, p, weights, riders):
    sv = {}
    hq = _mm(xn, wts['w_qkv_t'], "nt", "l0_in_qkv")
    gate = _mm(xn, wts['w_gate_t'], "nt", "l0_in_gate")
    ba = _mm(xn, wts['w_ba_t'], "nt", "l0_in_ba")
    u = _mm(xn, wts['w_u_t'], "nt", "l0_in_u")
    conv = p['conv_qkv']
    q = _qkv_pre_fwd(hq, conv, 0, 4, True, HEAD_A ** -0.5, "l0_q_pre")
    k = _qkv_pre_fwd(hq, conv, 4, 4, True, 1.0, "l0_k_pre")
    v = _qkv_pre_fwd(hq, conv, 8, 4, False, 1.0, "l0_v_pre")
    gates = _gates_fwd(ba, p['arow'], p['brow'], "l0_gates")
    o, tm_all, s_all = riders.run("l0_gdr_fwd", _gdr_fwd, q, k, v, gates)
    wts['w_glu'], wts['w_out'] = weights.full['w_glu'], weights.full['w_out']
    y_a = _onorm_fwd(o, gate, p['onorm_g'], "l0_onorm")
    bu = riders.run("l0_s5_bu", _mm_bd, u, p['b_in'], "nn")
    xs = riders.run("l0_s5_scan", _s5_scan_fwd, bu, p['a_row'])
    weights.gather_by_sequencer(GATHER_LAYER1, xs, "gather_layer1", GATHER_LAYER1_ID)
    yc = riders.run("l0_s5_cx", _mm_bd, xs, p['c_out'], "nn")
    yl, y_b = _glu_fwd(yc, u, p['d_row'], wts['w_glu'], p['b_glu'], "l0_glu")
    mixed = jnp.concatenate([y_a, y_b], axis=1)
    x1 = _mm(mixed, wts['w_out'], "nn", "l0_out", res=x)
    sv.update(hq=hq, gate=gate, ba=ba, u=u, q=q, k=k, v=v, gb=gates, o=o, tm=tm_all, s=s_all, xs=xs, yl=yl, mixed=mixed)
    return x1, sv


def _hybrid_bwd(dx1, xn, wts, p, sv, riders):
    gr = {}
    dmixed = _mm(dx1, wts['w_out'], "nt", "l0_out_dx", out_dtype=BF16)
    riders.grad('w_out', _as_pieces(_mm(sv['mixed'], dx1, "tn", "l0_out_dw", out_dtype=BF16)))
    dya, dyb = dmixed[:, :WIDTH_A], dmixed[:, WIDTH_A:]
    dyl, du_direct, dw_glu, gr['b_glu_b'], dd = _glu_bwd(
        sv['yl'], sv['u'], p['d_row'], wts['w_glu'], p['b_glu'], dyb, "l0_glu_bwd")
    riders.grad('w_glu', dw_glu.astype(BF16).reshape(N_DEV, -1, PACK_COLS))
    dxs = riders.run("l0_s5_cx_dx", _mm_bd, dyl, p['c_out'], "nt")
    dc_out = _mm_bd(sv['xs'], dyl, "tn", "l0_s5_cx_dw")
    lam, da_row = riders.run("l0_s5_scan_bwd", _s5_scan_bwd, dxs, sv['xs'], p['a_row'])
    du = _mm_bd(lam, p['b_in'], "nt", "l0_s5_bu_dx", res=du_direct, out_dtype=BF16)
    db_in = _mm_bd(sv['u'], lam, "tn", "l0_s5_bu_dw")
    gr['s5'] = (db_in, dc_out, da_row, dd)
    do, dgate, gr['onorm_g_a'] = _onorm_bwd(sv['o'], sv['gate'], p['onorm_g'], dya, "l0_onorm_bwd")
    dq, dk, dv, dgb = riders.run("l0_gdr_bwd", _gdr_bwd, sv['q'], sv['k'], sv['v'], sv['gb'], sv['tm'], sv['s'], do)
    conv = p['conv_qkv']
    dhq_q, dcw_q = _qkv_pre_bwd(sv['hq'], conv, dq, 0, 4, True, HEAD_A ** -0.5, "l0_q_pre_bwd")
    dhq_k, dcw_k = _qkv_pre_bwd(sv['hq'], conv, dk, 4, 4, True, 1.0, "l0_k_pre_bwd")
    dhq_v, dcw_v = _qkv_pre_bwd(sv['hq'], conv, dv, 8, 4, False, 1.0, "l0_v_pre_bwd")
    gr['conv_qkv_a'] = jnp.concatenate([dcw_q, dcw_k, dcw_v], axis=1)
    dhq = jnp.concatenate([dhq_q, dhq_k, dhq_v], axis=1)
    dba, da_log, ddt_bias = _gates_bwd(sv['ba'], p['arow'], p['brow'], dgb, "l0_gates_bwd")
    gr['a_log_a'], gr['dt_bias_a'] = da_log[:, 4:8], ddt_bias[:, 4:8]
    dw_qkv_t = _mm(dhq, xn, "tn", "l0_in_qkv_dw", out_dtype=BF16)
    dw_gate_t = _mm(dgate, xn, "tn", "l0_in_gate_dw", out_dtype=BF16)
    dw_ba_t = _mm(dba, xn, "tn", "l0_in_ba_dw", out_dtype=BF16)
    dw_u_t = _mm(du, xn, "tn", "l0_in_u_dw", out_dtype=BF16)
    dw_in_t = _as_pieces(jnp.concatenate([dw_qkv_t, dw_gate_t, dw_ba_t[:8], dw_u_t], axis=0))
    riders.grad('w_in_t', jnp.concatenate(
        [dw_in_t, jnp.zeros((N_DEV, dict(PIECES)['w_in_t'] - W_IN_PIECE, D_MODEL), BF16)], axis=1))
    dxn = riders.run("l0_in_qkv_dx", _mm, dhq, wts['w_qkv_t'], "nn")
    dxn = _mm(dgate, wts['w_gate_t'], "nn", "l0_in_gate_dx", res=dxn)
    dxn = _mm(dba, wts['w_ba_t'], "nn", "l0_in_ba_dx", res=dxn)
    dxn = riders.run("l0_in_u_dx", _mm, du, wts['w_u_t'], "nn", res=dxn)
    return dxn, gr


def _xa_fwd(x1, g, mem_n, wq, wkv_t, wo, tag, riders):
    xq = _rms_fwd(x1, g, BF16, tag + "_norm")
    q = _mm(xq, wq, "nn", tag + "_q", out_dtype=BF16)
    kv = _mm(mem_n, wkv_t, "nt", tag + "_kv", out_dtype=BF16)
    o = riders.run(tag + "_attn", _attn_fwd, q, kv)
    x2 = _mm(o, wo, "nn", tag + "_o", res=x1)
    return x2, dict(xq=xq, q=q, kv=kv, o=o)


def _xa_bwd(dx2, x1, g, mem_n, wq, wkv_t, wo, sv, tag, layer, riders):
    do = _mm(dx2, wo, "nt", tag + "_o_dx", out_dtype=BF16)
    riders.grad('wo%d' % layer, _as_pieces(_mm(sv['o'], dx2, "tn", tag + "_o_dw", out_dtype=BF16)))
    dq, dk, dv = _attn_bwd(sv['q'], sv['kv'], do, tag + "_attn_bwd")
    dkv = jnp.concatenate([dk, dv], axis=1).astype(BF16)
    dxq = _mm(dq, wq, "nt", tag + "_q_dx")
    riders.grad('wq%d' % layer, _as_pieces(_mm(sv['xq'], dq, "tn", tag + "_q_dw", out_dtype=BF16)))
    dmem_n = _mm(dkv, wkv_t, "nn", tag + "_kv_dx")
    riders.grad('wkv_t%d' % layer, _as_pieces(_mm(dkv, mem_n, "tn", tag + "_kv_dw", out_dtype=BF16)))
    dx1, dg = riders.run(tag + "_norm_bwd", _rms_bwd, x1, g, dxq, dx2)
    return dx1, dmem_n, dg


def _ffn_fwd(x2, g, w_up_t, conv, w_down, tag, riders):
    xf = _rms_fwd(x2, g, BF16, tag + "_norm")
    h = riders.run(tag + "_up", _mm, xf, w_up_t, "nt")
    a = riders.run(tag + "_act", _ffn_act_fwd, h, conv)
    x3 = _mm(a, w_down, "nn", tag + "_down", res=x2)
    return x3, dict(xf=xf, h=h, a=a)


def _ffn_bwd(dx3, x2, g, w_up_t, conv, w_down, sv, tag, layer, riders):
    da = _mm(dx3, w_down, "nt", tag + "_down_dx")
    riders.grad('down%d' % layer, _as_pieces(_mm(sv['a'], dx3, "tn", tag + "_down_dw", out_dtype=BF16)))
    dh, dconv = riders.run(tag + "_act_bwd", _ffn_act_bwd, sv['h'], conv, da)
    dxf = riders.run(tag + "_up_dx", _mm, dh, w_up_t, "nn")
    dw_up_t = riders.run(tag + "_up_dw", _mm, dh, sv['xf'], "tn", out_dtype=BF16)
    riders.grad('up_t%d' % layer, _as_pieces(dw_up_t))
    dx2, dg = riders.run(tag + "_norm_bwd", _rms_bwd, x2, g, dxf, dx3)
    return dx2, dconv, dg


BIG_NAMES, SMALL_NAMES, REP_NAMES = list(BIG_SHARDED), list(SMALL_SHARDED), list(REPLICATED)
SMALL_SIZES = [int(np.prod(_shard_shape(*SMALL_SHARDED[n]))) for n in SMALL_NAMES]


PIECES = [('w_in_t', 384), ('w_glu', 32), ('w_out', 128), ('pool_w', 32), ('wq0', 128), ('wq1', 128),
          ('wkv_t0', 256), ('wkv_t1', 256), ('wo0', 128), ('wo1', 128), ('up_t0', 704), ('up_t1', 704),
          ('down0', 352), ('down1', 352)]
W_IN_ROWS = 4 * WIDTH_A + 2 * N_HEADS_A + SSM_WIDTH
W_IN_PIECE = W_IN_ROWS // N_DEV


def _row_tile(rows):
    return max(t for t in range(16, min(rows, 512) + 1, 16) if rows % t == 0)


class _Riders:
    def __init__(self):
        self.waiting = {}
        self.deferred = {}
        self.grads = {}
        self.groups = []
        self.reduced = {}

    def add(self, host, comm, then):
        self.waiting.setdefault(host, []).append((comm, then))

    def after(self, marker, then):
        self.deferred.setdefault(marker, []).append(then)

    def mark(self, name, out=None):
        for cont in self.deferred.pop(name, []):
            step = cont()
            if step is not None:
                values, then = step
                out, values = lax.optimization_barrier((out, values))
                then(values)
        return out

    def run(self, name, fn, *args, **kw):
        riders = self.waiting.pop(name, [])
        if not riders:
            out = fn(*args, name=name, **kw)
        else:
            out, couts = fn(*args, name=name, comm=[c for c, _ in riders], **kw)
            for (_, then), got in zip(riders, couts):
                then(got)
        return self.mark(name, out)

    def grad(self, key, pieces):
        self.grads[key] = pieces
        for group in [g for g in self.groups if all(k in self.grads for k in g[1])]:
            self.groups.remove(group)
            self._reduce(*group)

    def _reduce(self, name, keys, pair_marker, sum_marker):
        arrays = [self.grads[k] for k in keys]
        rows = sum(a.shape[1] for a in arrays)
        tile = _row_tile(rows)
        from_sibling = _sequencer_swap(arrays, name + "_to_sibling")

        def after_swap():
            if len(arrays) == 1:
                chip_sums = _pair_sum_pieces(arrays[0], from_sibling, name + "_pair_sum", tr=tile)
            else:
                core = lax.axis_index("c")
                keep = jnp.concatenate(
                    [lax.dynamic_index_in_dim(a.reshape(4, 2, a.shape[1], PACK_COLS), core, 1, keepdims=False)
                     for a in arrays], axis=1)
                chip_sums = _pair_sum(keep, from_sibling, name + "_pair_sum", tr=tile)

            def exchange_among_chips(chip_sums):
                from_chips = _sequencer_chips(chip_sums, name + "_to_chips")

                def store(total):
                    off = 0
                    for k, a in zip(keys, arrays):
                        self.reduced[k] = total[off:off + a.shape[1]]
                        off += a.shape[1]

                self.after(sum_marker, lambda: (
                    _chip_sum(chip_sums, from_chips, name + "_chip_sum", tr=tile), store))

            return chip_sums, exchange_among_chips

        self.after(pair_marker, after_swap)


class _Weights:
    def __init__(self, inp):
        bf = lambda a: a.astype(BF16)
        local = {'w_in_t': bf(inp['w_in_ab'][0]).T, 'w_glu': bf(inp['w_glu_b'][0]), 'w_out': bf(inp['w_out_ab'][0]),
                 'pool_w': bf(inp['pool_w'][0]),
                 'small': _pack([inp[n] for n in SMALL_NAMES])}
        for l in range(2):
            local['wq%d' % l] = bf(inp['xa_wq'][l])
            local['wkv_t%d' % l] = bf(inp['xa_wkv'][l]).T
            local['wo%d' % l] = bf(inp['xa_wo'][l])
            local['up_t%d' % l] = bf(inp['ffn_w_up'][l]).T
            local['down%d' % l] = bf(inp['ffn_w_down'][l])
        self.local, self.full = local, {}

    def plan(self, keys):
        return _gather_comm([self.local[k] for k in keys])

    def gather_by_sequencer(self, keys, after, name, collective_id):
        arrays = [self.local[k] for k in keys]
        tie = (after.reshape(-1)[0] * 0.0).astype(arrays[0].dtype)
        arrays[0] = arrays[0] + tie
        self.land(keys, _sequencer_gather(arrays, name, collective_id))

    def land(self, keys, gathered):
        for k, g in zip(keys, gathered):
            if k == 'small':
                off = 0
                for n, size in zip(SMALL_NAMES, SMALL_SIZES):
                    self.full[n] = _merge_shards(g.reshape(N_DEV, -1)[:, off:off + size], *SMALL_SHARDED[n])
                    off += size
            elif k == 'pool_w':
                self.full[k] = jnp.swapaxes(g, 0, 1).reshape(len(POOL_WINDOWS), POOL_GROUP, POOL_GROUP)
            else:
                self.full[k] = g.reshape(N_DEV * g.shape[1], g.shape[2])


GATHER_FIRST = ['w_in_t', 'small']
GATHER_LAYER0 = ['w_glu', 'w_out', 'wq0', 'wkv_t0', 'wo0', 'down0', 'up_t0']
GATHER_LAYER1 = ['pool_w', 'wq1', 'wkv_t1', 'wo1', 'up_t1', 'down1']
GATHER_LAYER0_ID, GATHER_LAYER1_ID = 7, 8
GRAD_RIDES = [('g_down1', ['down1'], 'l1_ffn_up_dx', 'l1_xa_norm_bwd'),
              ('g_up1', ['up_t1'], 'l1_xa_norm_bwd', 'l0_ffn_up_dx'),
              ('g_xa1', ['wq1', 'wkv_t1', 'wo1', 'pool_w'], 'l0_ffn_up_dx', 'l0_xa_norm_bwd'),
              ('g_down0', ['down0'], 'l0_ffn_up_dx', 'l0_s5_scan_bwd'),
              ('g_l0', ['up_t0', 'wq0', 'wkv_t0', 'wo0'], 'l0_s5_cx_dx', 'l0_in_u_dx'),
              ('g_out', ['w_out', 'w_glu'], 'l0_s5_scan_bwd', 'l0_in_u_dx'),
              ('g_in', ['w_in_t'], 'l0_in_u_dx', 'adamw_pool_w')]


def _local_step(inp):
    f32_of = lambda n: inp[n].astype(F32)
    weights = _Weights(inp)
    riders = _Riders()
    riders.groups = list(GRAD_RIDES)
    full = weights.full
    weights.land(GATHER_FIRST, _comm_only(weights.plan(GATHER_FIRST), "gather_first"))
    weights.gather_by_sequencer(GATHER_LAYER0, full['w_in_t'], "gather_layer0", GATHER_LAYER0_ID)
    w_in_t = full['w_in_t']
    wts0 = dict(w_qkv_t=w_in_t[:3 * WIDTH_A], w_gate_t=w_in_t[3 * WIDTH_A:4 * WIDTH_A],
                w_ba_t=jnp.concatenate([w_in_t[4 * WIDTH_A:4 * WIDTH_A + 8], jnp.zeros((LANE - 8, D_MODEL), BF16)], 0),
                w_u_t=w_in_t[4 * WIDTH_A + 8:])
    lb_disc, disc_vjp = jax.vjp(_s5_discretise, f32_of('ssm_lambda_re')[0], f32_of('ssm_lambda_im')[0],
                                f32_of('ssm_b_re')[0], f32_of('ssm_b_im')[0], f32_of('ssm_log_dt')[0])
    b_in, c_out, a_row = _s5_matrices(*lb_disc, f32_of('ssm_c_re')[0], f32_of('ssm_c_im')[0])
    zeros4 = jnp.zeros((1, 4), F32)
    p0 = dict(conv_qkv=full['conv_qkv_a'][0], onorm_g=f32_of('onorm_g_a'),
              arow=jnp.concatenate([zeros4, f32_of('a_log_a'), jnp.zeros((1, LANE - 8), F32)], 1),
              brow=jnp.concatenate([zeros4, f32_of('dt_bias_a'), jnp.zeros((1, LANE - 8), F32)], 1),
              b_in=b_in.astype(BF16), c_out=c_out.astype(BF16), a_row=a_row,
              d_row=f32_of('ssm_d').reshape(1, SSM_WIDTH), b_glu=f32_of('b_glu_b'))

    x0 = inp['x'][0]
    mem_n = _rms_fwd(inp['mem'][0], inp['norm_mem_g'], BF16, "mem_norm")
    xn0 = _rms_fwd(x0, inp['norm_mix_g'][0], BF16, "l0_mix_norm")
    x1, sv_mix0 = _hybrid_fwd(xn0, x0, wts0, p0, weights, riders)
    x2, sv_xa0 = _xa_fwd(x1, inp['norm_xa_g'][0], mem_n, full['wq0'], full['wkv_t0'], full['wo0'], "l0_xa", riders)
    x3, sv_ffn0 = _ffn_fwd(x2, inp['norm_ffn_g'][0], full['up_t0'], full['ffn_conv'][0], full['down0'], "l0_ffn", riders)
    xn1 = _rms_fwd(x3, inp['norm_mix_g'][1], F32, "l1_mix_norm")
    x4 = _pool_fwd(xn1, full['pool_w'], full['pool_scale'], x3, "l1_pool")
    x5, sv_xa1 = _xa_fwd(x4, inp['norm_xa_g'][1], mem_n, full['wq1'], full['wkv_t1'], full['wo1'], "l1_xa", riders)
    x6, sv_ffn1 = _ffn_fwd(x5, inp['norm_ffn_g'][1], full['up_t1'], full['ffn_conv'][1], full['down1'], "l1_ffn", riders)
    loss_part, dx6, dg_final = _loss_head(x6, inp['norm_final_g'], inp['loss_target'][0], "loss_head")

    dx5, dconv1, dg_ffn1 = _ffn_bwd(dx6, x5, inp['norm_ffn_g'][1], full['up_t1'], full['ffn_conv'][1], full['down1'],
                                    sv_ffn1, "l1_ffn", 1, riders)
    dx4, dmem1, dg_xa1 = _xa_bwd(dx5, x4, inp['norm_xa_g'][1], mem_n, full['wq1'], full['wkv_t1'], full['wo1'],
                                 sv_xa1, "l1_xa", 1, riders)
    dxn1, dpool_w, dpool_scale = _pool_bwd(xn1, full['pool_w'], full['pool_scale'], dx4, "l1_pool_bwd")
    pool_pieces = jnp.swapaxes(dpool_w.astype(BF16).reshape(len(POOL_WINDOWS), N_DEV, -1, POOL_GROUP), 0, 1)
    riders.grad('pool_w', pool_pieces.reshape(N_DEV, -1, PACK_COLS))
    dx3, dg_mix1 = riders.run("l1_mix_norm_bwd", _rms_bwd, x3, inp['norm_mix_g'][1], dxn1, dx4)
    dx2, dconv0, dg_ffn0 = _ffn_bwd(dx3, x2, inp['norm_ffn_g'][0], full['up_t0'], full['ffn_conv'][0], full['down0'],
                                    sv_ffn0, "l0_ffn", 0, riders)
    dx1, dmem0, dg_xa0 = _xa_bwd(dx2, x1, inp['norm_xa_g'][0], mem_n, full['wq0'], full['wkv_t0'], full['wo0'],
                                 sv_xa0, "l0_xa", 0, riders)
    dxn0, g_mix0 = _hybrid_bwd(dx1, xn0, wts0, p0, sv_mix0, riders)
    grad_x, dg_mix0 = _rms_bwd(x0, inp['norm_mix_g'][0], dxn0, dx1, "l0_mix_norm_bwd")
    _, dg_mem = _rms_bwd(inp['mem'][0], inp['norm_mem_g'], dmem0 + dmem1, None, "mem_norm_bwd")
    assert not riders.groups and not riders.waiting and all(k.startswith("adamw_") for k in riders.deferred), (
        riders.groups, list(riders.waiting), list(riders.deferred))

    db_in, dc_out, da_row, dd = g_mix0['s5']
    dlb_re, dlb_im, dbb_re, dbb_im, dc_re, dc_im = _s5_matrix_grads(db_in, dc_out, da_row)
    dlam_re, dlam_im, dbr, dbi, dlog_dt = disc_vjp((dlb_re, dlb_im, dbb_re, dbb_im))

    rep_grads = {
        'norm_mix_g': jnp.concatenate([dg_mix0, dg_mix1], 0), 'norm_xa_g': jnp.concatenate([dg_xa0, dg_xa1], 0),
        'norm_ffn_g': jnp.concatenate([dg_ffn0, dg_ffn1], 0), 'norm_mem_g': dg_mem.reshape(-1),
        'norm_final_g': dg_final.reshape(-1), 'a_log_a': g_mix0['a_log_a'], 'dt_bias_a': g_mix0['dt_bias_a'],
        'onorm_g_a': g_mix0['onorm_g_a'], 'ssm_lambda_re': dlam_re[None], 'ssm_lambda_im': dlam_im[None],
        'ssm_b_re': dbr[None], 'ssm_b_im': dbi[None], 'ssm_c_re': dc_re[None], 'ssm_c_im': dc_im[None],
        'ssm_d': dd.reshape(1, N_GROUPS, SSM_GROUP), 'ssm_log_dt': dlog_dt[None], 'b_glu_b': g_mix0['b_glu_b']}
    small_grads = {'conv_qkv_a': g_mix0['conv_qkv_a'][None], 'pool_scale': dpool_scale,
                   'ffn_conv': jnp.stack([dconv0, dconv1])}
    return loss_part, grad_x, riders, rep_grads, small_grads


ADAMW_ORDER = ['ffn_w_up', 'ffn_w_down', 'xa_wkv', 'xa_wq', 'xa_wo', 'w_out_ab', 'w_glu_b', 'pool_w', 'w_in_ab']


def _update(inp, loss_part, grad_x, riders, rep_grads, small_grads):
    dev = _device_index()
    misc_local = _pack([rep_grads[n] for n in REP_NAMES] + [small_grads[n] for n in SMALL_NAMES] + [loss_part])
    (misc_all,) = _sequencer_gather([misc_local], "gather_small_grads", GATHER_LAYER0_ID)
    piece = lambda key: riders.reduced[key]
    both = lambda name: jnp.stack([piece(name + '0'), piece(name + '1')])
    swap = lambda a: jnp.swapaxes(a, -1, -2)
    reduced = {'w_in_ab': lambda: piece('w_in_t')[:W_IN_PIECE][None],
               'w_glu_b': lambda: piece('w_glu').reshape(inp['w_glu_b'].shape),
               'w_out_ab': lambda: piece('w_out')[None], 'pool_w': lambda: piece('pool_w').reshape(inp['pool_w'].shape),
               'xa_wq': lambda: both('wq'), 'xa_wkv': lambda: both('wkv_t'), 'xa_wo': lambda: both('wo'),
               'ffn_w_up': lambda: both('up_t'), 'ffn_w_down': lambda: both('down')}
    transposed = ('w_in_ab', 'xa_wkv', 'ffn_w_up')
    grads, upd = {}, {}
    assert sorted(ADAMW_ORDER) == sorted(BIG_NAMES)
    for n in ADAMW_ORDER:
        fix = swap if n in transposed else (lambda a: a)
        g = reduced[n]()
        out = riders.run("adamw_" + n, _adamw, fix(inp[n]), g, fix(inp['m_' + n]), fix(inp['v_' + n]))
        upd[n], grads[n] = tuple(fix(o) for o in out), fix(g)
    assert not riders.waiting and not riders.deferred, (list(riders.waiting), list(riders.deferred))
    misc_sum = _sum_leading(misc_all, "small_grads_sum")
    misc = _unpack(misc_sum, [inp[n].shape for n in REP_NAMES] + [SMALL_SHARDED[n][0] for n in SMALL_NAMES] + [()])
    loss = misc.pop()
    for n, g in zip(REP_NAMES, misc):
        grads[n] = g
    for n, g in zip(SMALL_NAMES, misc[len(REP_NAMES):]):
        grads[n] = lax.dynamic_index_in_dim(_split_shards(g, SMALL_SHARDED[n][1]), dev, 0, keepdims=False
                                            ).reshape(inp[n].shape)
    tiny_names = REP_NAMES + SMALL_NAMES
    rep_total = sum(int(np.prod(inp[n].shape)) for n in REP_NAMES)
    packs = [_pack([inp[prefix + n] for n in tiny_names]) for prefix in ('', 'm_', 'v_')]
    g_pack = _pack([misc_sum.reshape(-1)[:rep_total]] + [grads[n] for n in SMALL_NAMES])
    tiny_out = [_unpack(o, [inp[n].shape for n in tiny_names])
                for o in _adamw(packs[0], g_pack, packs[1], packs[2], "adamw_small")]
    for i, n in enumerate(tiny_names):
        upd[n] = tuple(o[i] for o in tiny_out)

    outs = [loss, grad_x[None]]
    outs += [grads[n] for n in WEIGHT_NAMES]
    for i in range(3):
        outs += [upd[n][i] for n in WEIGHT_NAMES]
    return tuple(outs)


def _step(inp):
    loss_part, grad_x, riders, rep_grads, small_grads = _local_step(inp)
    return _update(inp, loss_part, grad_x, riders, rep_grads, small_grads)


INPUT_NAMES = (['x', 'mem'] + WEIGHT_NAMES + ['loss_target'] + ['m_' + n for n in WEIGHT_NAMES]
               + ['v_' + n for n in WEIGHT_NAMES])


def kernel(x, mem, norm_mix_g, norm_xa_g, norm_ffn_g, norm_mem_g, norm_final_g, w_in_ab, conv_qkv_a, a_log_a, dt_bias_a, onorm_g_a, ssm_lambda_re, ssm_lambda_im, ssm_b_re, ssm_b_im, ssm_c_re, ssm_c_im, ssm_d, ssm_log_dt, w_glu_b, b_glu_b, w_out_ab, pool_w, pool_scale, xa_wq, xa_wkv, xa_wo, ffn_w_up, ffn_conv, ffn_w_down, loss_target, m_norm_mix_g, m_norm_xa_g, m_norm_ffn_g, m_norm_mem_g, m_norm_final_g, m_w_in_ab, m_conv_qkv_a, m_a_log_a, m_dt_bias_a, m_onorm_g_a, m_ssm_lambda_re, m_ssm_lambda_im, m_ssm_b_re, m_ssm_b_im, m_ssm_c_re, m_ssm_c_im, m_ssm_d, m_ssm_log_dt, m_w_glu_b, m_b_glu_b, m_w_out_ab, m_pool_w, m_pool_scale, m_xa_wq, m_xa_wkv, m_xa_wo, m_ffn_w_up, m_ffn_conv, m_ffn_w_down, v_norm_mix_g, v_norm_xa_g, v_norm_ffn_g, v_norm_mem_g, v_norm_final_g, v_w_in_ab, v_conv_qkv_a, v_a_log_a, v_dt_bias_a, v_onorm_g_a, v_ssm_lambda_re, v_ssm_lambda_im, v_ssm_b_re, v_ssm_b_im, v_ssm_c_re, v_ssm_c_im, v_ssm_d, v_ssm_log_dt, v_w_glu_b, v_b_glu_b, v_w_out_ab, v_pool_w, v_pool_scale, v_xa_wq, v_xa_wkv, v_xa_wo, v_ffn_w_up, v_ffn_conv, v_ffn_w_down):
    args = (x, mem, norm_mix_g, norm_xa_g, norm_ffn_g, norm_mem_g, norm_final_g, w_in_ab, conv_qkv_a, a_log_a, dt_bias_a, onorm_g_a, ssm_lambda_re, ssm_lambda_im, ssm_b_re, ssm_b_im, ssm_c_re, ssm_c_im, ssm_d, ssm_log_dt, w_glu_b, b_glu_b, w_out_ab, pool_w, pool_scale, xa_wq, xa_wkv, xa_wo, ffn_w_up, ffn_conv, ffn_w_down, loss_target, m_norm_mix_g, m_norm_xa_g, m_norm_ffn_g, m_norm_mem_g, m_norm_final_g, m_w_in_ab, m_conv_qkv_a, m_a_log_a, m_dt_bias_a, m_onorm_g_a, m_ssm_lambda_re, m_ssm_lambda_im, m_ssm_b_re, m_ssm_b_im, m_ssm_c_re, m_ssm_c_im, m_ssm_d, m_ssm_log_dt, m_w_glu_b, m_b_glu_b, m_w_out_ab, m_pool_w, m_pool_scale, m_xa_wq, m_xa_wkv, m_xa_wo, m_ffn_w_up, m_ffn_conv, m_ffn_w_down, v_norm_mix_g, v_norm_xa_g, v_norm_ffn_g, v_norm_mem_g, v_norm_final_g, v_w_in_ab, v_conv_qkv_a, v_a_log_a, v_dt_bias_a, v_onorm_g_a, v_ssm_lambda_re, v_ssm_lambda_im, v_ssm_b_re, v_ssm_b_im, v_ssm_c_re, v_ssm_c_im, v_ssm_d, v_ssm_log_dt, v_w_glu_b, v_b_glu_b, v_w_out_ab, v_pool_w, v_pool_scale, v_xa_wq, v_xa_wkv, v_xa_wo, v_ffn_w_up, v_ffn_conv, v_ffn_w_down)
    return _step(dict(zip(INPUT_NAMES, args)))
```

```python
import functools
import math

import numpy as np
import jax
import jax.numpy as jnp
from jax import lax
from jax.experimental import pallas as pl
from jax.experimental.pallas import tpu as pltpu
from jax.experimental.pallas import tpu_sc as plsc

F32, BF16 = jnp.float32, jnp.bfloat16
HIGH, HIGHEST = lax.Precision.HIGH, lax.Precision.HIGHEST
MESH = pl.DeviceIdType.MESH

N_DEV = 8
SEQ, D_MODEL, MEM_LEN = 2048, 1024, 256
WIDTH_A, N_HEADS_A, HEAD_A, CONV_A = 512, 4, 128, 4
GDR_CHUNK = 128
SSM_WIDTH, SSM_GROUP, N_GROUPS, SSM_STATE = 512, 16, 32, 64
SSM_CH = N_GROUPS * SSM_STATE
SCAN_CB = 512
POOL_WINDOWS = (2, 4, 8, 16)
POOL_GROUP = 256
N_HEADS_X, HEAD_X = 4, 256
D_FF, CONV_FFN = 2816, 3
RMS_EPS = 1e-6
ADAM_LR, ADAM_B1, ADAM_B2, ADAM_EPS, ADAM_WD, ADAM_STEP = 0.001, 0.9, 0.999, 1e-08, 0.01, 10
LANE = 128
PACK_COLS = 1024
VMEM_LIMIT_BYTES = 56 * 1024 * 1024


def _params(sem=None):
    return pltpu.CompilerParams(dimension_semantics=sem, vmem_limit_bytes=VMEM_LIMIT_BYTES)


class Comm:
    def __init__(self, inputs, out_shapes, sems, start, end, mid=None):
        self.inputs, self.out_shapes, self.sems = list(inputs), list(out_shapes), list(sems)
        self.start, self.mid, self.end = start, mid, end


def _merge_comms(comms):
    comms = [c for c in comms if c is not None]
    if not comms:
        return None, []
    bounds, ni, no, ns = [], 0, 0, 0
    for c in comms:
        bounds.append((ni, no, ns))
        ni, no, ns = ni + len(c.inputs), no + len(c.out_shapes), ns + len(c.sems)

    def phase(which):
        def run(ins, outs, sems):
            for c, (i0, o0, s0) in zip(comms, bounds):
                fn = getattr(c, which)
                if fn is not None:
                    fn(ins[i0:i0 + len(c.inputs)], outs[o0:o0 + len(c.out_shapes)], sems[s0:s0 + len(c.sems)])
        return run

    merged = Comm([a for c in comms for a in c.inputs], [s for c in comms for s in c.out_shapes],
                  [s for c in comms for s in c.sems], phase("start"), phase("end"), phase("mid"))
    return merged, [(o0, o0 + len(c.out_shapes)) for c, (_, o0, _) in zip(comms, bounds)]


def _call(body, *, name, grid, in_specs, out_specs, out_shape, args, scratch_shapes=(), sem=None, comm=None):
    single = not isinstance(out_shape, (list, tuple))
    out_specs_l = [out_specs] if single else list(out_specs)
    out_shape_l = [out_shape] if single else list(out_shape)
    scratch_shapes = list(scratch_shapes)
    merged, spans = _merge_comms(comm if isinstance(comm, (list, tuple)) else [comm])
    if merged is None:
        outs = pl.pallas_call(body, name=name, grid=grid, in_specs=list(in_specs), out_specs=out_specs_l,
                              out_shape=out_shape_l, scratch_shapes=scratch_shapes, compiler_params=_params(sem))(*args)
        outs = outs[0] if single else outs
        return outs if comm is None else (outs, [])
    n_in, n_out, n_scr = len(in_specs), len(out_specs_l), len(scratch_shapes)
    ci, co = len(merged.inputs), len(merged.out_shapes)
    total = int(np.prod(grid))

    def wrapped(*refs):
        ins, cins = refs[:n_in], refs[n_in:n_in + ci]
        outs, couts = refs[n_in + ci:n_in + ci + n_out], refs[n_in + ci + n_out:n_in + ci + n_out + co]
        scr, csems = refs[n_in + ci + n_out + co:n_in + ci + n_out + co + n_scr], refs[n_in + ci + n_out + co + n_scr:]
        lin = pl.program_id(0)
        for d in range(1, len(grid)):
            lin = lin * grid[d] + pl.program_id(d)
        pl.when(lin == 0)(lambda: merged.start(cins, couts, csems))
        body(*ins, *outs, *scr)
        mid_step = min((3 * total) // 4, total - 1)
        pl.when(lin == mid_step)(lambda: merged.mid(cins, couts, csems))
        pl.when(lin == total - 1)(lambda: merged.end(cins, couts, csems))

    any_spec = pl.BlockSpec(memory_space=pl.ANY)
    res = pl.pallas_call(
        wrapped, name=name, grid=grid, in_specs=list(in_specs) + [any_spec] * ci,
        out_specs=out_specs_l + [any_spec] * co, out_shape=out_shape_l + merged.out_shapes,
        scratch_shapes=scratch_shapes + merged.sems,
        compiler_params=_params(("arbitrary",) * len(grid)))(*args, *merged.inputs)
    outs, couts = res[:n_out], res[n_out:]
    return (outs[0] if single else list(outs)), [list(couts[a:b]) for a, b in spans]


def _comm_only(comm, name):
    def body():
        pass

    _, couts = _call(body, name=name, grid=(1,), in_specs=[], out_specs=[], out_shape=[], args=[], comm=comm)
    return couts[0]


def _tile(dim, pref):
    best = None
    for t in range(LANE, min(dim, pref) + 1, LANE):
        if dim % t == 0:
            best = t
    return best if best is not None else dim


MM_VMEM_BUDGET = 40 * 1024 * 1024


def _mm_tiles(m, n, k, a_bytes, b_bytes, o_bytes, r_bytes):
    for tk in (k, _tile(k, 2048), _tile(k, 1024), _tile(k, 512)):
        for tm, tn in ((1024, 1536), (1024, 1024), (1024, 512), (512, 512), (256, 512), (256, 256)):
            tm, tn = _tile(m, tm), _tile(n, tn)
            acc = 0 if tk == k else tm * tn * 4
            need = 2 * (tm * tk * a_bytes + tk * tn * b_bytes + tm * tn * (o_bytes + r_bytes)) + acc
            if need <= MM_VMEM_BUDGET:
                return tm, tn, tk
    raise ValueError("no matmul tiling fits VMEM")


def _mm(a, b, mode, name, out_dtype=F32, res=None, comm=None):
    if mode == "nn":
        (m, k), n = a.shape, b.shape[1]
    elif mode == "nt":
        (m, k), n = a.shape, b.shape[0]
    else:
        (k, m), n = a.shape, b.shape[1]
    tm, tn, tk = _mm_tiles(m, n, k, a.dtype.itemsize, b.dtype.itemsize, jnp.dtype(out_dtype).itemsize,
                           0 if res is None else res.dtype.itemsize)
    nk = k // tk
    dims = {"nn": ((1,), (0,)), "nt": ((1,), (1,)), "tn": ((0,), (0,))}[mode]

    def body(*refs):
        if res is None:
            a_ref, b_ref, o_ref = refs[:3]
            r_ref = None
        else:
            a_ref, b_ref, r_ref, o_ref = refs[:4]
        part = lax.dot_general(a_ref[...].astype(BF16), b_ref[...].astype(BF16), (dims, ((), ())),
                               preferred_element_type=F32)

        def finish(out):
            if r_ref is not None:
                out = out + r_ref[...].astype(F32)
            o_ref[...] = out.astype(out_dtype)

        if nk == 1:
            finish(part)
            return
        acc = refs[-1]
        kk = pl.program_id(2)

        @pl.when(kk == 0)
        def _():
            acc[...] = part

        @pl.when(kk > 0)
        def _():
            acc[...] += part

        @pl.when(kk == nk - 1)
        def _():
            finish(acc[...])

    a_spec = (pl.BlockSpec((tk, tm), lambda i, j, q: (q, i)) if mode == "tn"
              else pl.BlockSpec((tm, tk), lambda i, j, q: (i, q)))
    b_spec = (pl.BlockSpec((tn, tk), lambda i, j, q: (j, q)) if mode == "nt"
              else pl.BlockSpec((tk, tn), lambda i, j, q: (q, j)))
    o_spec = pl.BlockSpec((tm, tn), lambda i, j, q: (i, j))
    in_specs, args = [a_spec, b_spec], [a, b]
    if res is not None:
        in_specs.append(o_spec)
        args.append(res)
    return _call(body, name=name, grid=(m // tm, n // tn, nk), in_specs=in_specs, out_specs=o_spec,
                 out_shape=jax.ShapeDtypeStruct((m, n), out_dtype),
                 scratch_shapes=[] if nk == 1 else [pltpu.VMEM((tm, tn), F32)],
                 sem=("parallel", "parallel", "arbitrary"), args=args, comm=comm)


def _mm_bd(a, b, mode, name, out_dtype=F32, res=None, comm=None, tm=1024):
    if mode == "tn":
        k = a.shape[0]
        nb = min(a.shape[1], b.shape[1]) // LANE
        ma, n = a.shape[1] // nb, b.shape[1] // nb

        def body(a_ref, b_ref, o_ref):
            o_ref[0] = lax.dot_general(a_ref[...].astype(BF16), b_ref[...].astype(BF16), (((0,), (0,)), ((), ())),
                                       preferred_element_type=F32).astype(out_dtype)

        return _call(body, name=name, grid=(nb,),
                     in_specs=[pl.BlockSpec((k, ma), lambda j: (0, j)), pl.BlockSpec((k, n), lambda j: (0, j))],
                     out_specs=pl.BlockSpec((1, ma, n), lambda j: (j, 0, 0)),
                     out_shape=jax.ShapeDtypeStruct((nb, ma, n), out_dtype), sem=("parallel",), args=(a, b), comm=comm)
    m = a.shape[0]
    nb = b.shape[0]
    ka = a.shape[1] // nb
    n = b.shape[2] if mode == "nn" else b.shape[1]
    tm = _tile(m, tm)
    dims = ((1,), (0,)) if mode == "nn" else ((1,), (1,))

    def body(*refs):
        if res is None:
            a_ref, b_ref, o_ref = refs
            r_ref = None
        else:
            a_ref, b_ref, r_ref, o_ref = refs
        out = lax.dot_general(a_ref[...].astype(BF16), b_ref[0].astype(BF16), (dims, ((), ())),
                              preferred_element_type=F32)
        if r_ref is not None:
            out = out + r_ref[...].astype(F32)
        o_ref[...] = out.astype(out_dtype)

    o_spec = pl.BlockSpec((tm, n), lambda i, j: (i, j))
    in_specs = [pl.BlockSpec((tm, ka), lambda i, j: (i, j)), pl.BlockSpec((1,) + b.shape[1:], lambda i, j: (j, 0, 0))]
    args = [a, b]
    if res is not None:
        in_specs.append(o_spec)
        args.append(res)
    return _call(body, name=name, grid=(m // tm, nb), in_specs=in_specs, out_specs=o_spec,
                 out_shape=jax.ShapeDtypeStruct((m, nb * n), out_dtype), sem=("parallel", "parallel"),
                 args=args, comm=comm)


def _rms_fwd(x, g, out_dtype, name, tr=256):
    rows, d = x.shape

    def body(x_ref, g_ref, o_ref):
        xv = x_ref[...]
        r = lax.rsqrt(jnp.mean(xv * xv, axis=-1, keepdims=True) + RMS_EPS)
        o_ref[...] = (xv * r * g_ref[...]).astype(out_dtype)

    return pl.pallas_call(
        body, name=name, grid=(rows // tr,),
        in_specs=[pl.BlockSpec((tr, d), lambda i: (i, 0)), pl.BlockSpec((1, d), lambda i: (0, 0))],
        out_specs=pl.BlockSpec((tr, d), lambda i: (i, 0)), out_shape=jax.ShapeDtypeStruct((rows, d), out_dtype),
        compiler_params=_params(("parallel",)))(x, g.reshape(1, d))


def _rms_bwd(x, g, dy, dres, name, tr=256, comm=None):
    rows, d = x.shape

    def body(*refs):
        if dres is None:
            x_ref, g_ref, dy_ref, dx_ref, dg_ref = refs
            r_ref = None
        else:
            x_ref, g_ref, dy_ref, r_ref, dx_ref, dg_ref = refs

        @pl.when(pl.program_id(0) == 0)
        def _():
            dg_ref[...] = jnp.zeros_like(dg_ref)

        xv, dyv = x_ref[...], dy_ref[...].astype(F32)
        r = lax.rsqrt(jnp.mean(xv * xv, axis=-1, keepdims=True) + RMS_EPS)
        xh = xv * r
        dyg = dyv * g_ref[...]
        dx = r * (dyg - xh * jnp.mean(dyg * xh, axis=-1, keepdims=True))
        if r_ref is not None:
            dx = dx + r_ref[...]
        dx_ref[...] = dx
        dg_ref[...] += jnp.sum(dyv * xh, axis=0, keepdims=True)

    blk = pl.BlockSpec((tr, d), lambda i: (i, 0))
    vec = pl.BlockSpec((1, d), lambda i: (0, 0))
    in_specs, args = [blk, vec, blk], [x, g.reshape(1, d), dy]
    if dres is not None:
        in_specs.append(blk)
        args.append(dres)
    return _call(
        body, name=name, grid=(rows // tr,), in_specs=in_specs, out_specs=[blk, vec],
        out_shape=[jax.ShapeDtypeStruct((rows, d), F32), jax.ShapeDtypeStruct((1, d), F32)],
        sem=("arbitrary",), args=args, comm=comm)


def _loss_head(x, g, target, name, tr=256):
    rows, d = x.shape

    def body(x_ref, g_ref, t_ref, loss_ref, dx_ref, dg_ref):
        @pl.when(pl.program_id(0) == 0)
        def _():
            dg_ref[...] = jnp.zeros_like(dg_ref)
            loss_ref[...] = jnp.zeros_like(loss_ref)

        xv = x_ref[...]
        r = lax.rsqrt(jnp.mean(xv * xv, axis=-1, keepdims=True) + RMS_EPS)
        xh = xv * r
        err = xh * g_ref[...] - t_ref[...]
        loss_ref[...] += 0.5 * jnp.sum(jnp.mean(err * err, axis=-1, keepdims=True), keepdims=True)
        dyv = err * (1.0 / d)
        dyg = dyv * g_ref[...]
        dx_ref[...] = r * (dyg - xh * jnp.mean(dyg * xh, axis=-1, keepdims=True))
        dg_ref[...] += jnp.sum(dyv * xh, axis=0, keepdims=True)

    blk = pl.BlockSpec((tr, d), lambda i: (i, 0))
    vec = pl.BlockSpec((1, d), lambda i: (0, 0))
    return pl.pallas_call(
        body, name=name, grid=(rows // tr,), in_specs=[blk, vec, blk],
        out_specs=[pl.BlockSpec((1, 1), lambda i: (0, 0)), blk, vec],
        out_shape=[jax.ShapeDtypeStruct((1, 1), F32), jax.ShapeDtypeStruct((rows, d), F32),
                   jax.ShapeDtypeStruct((1, d), F32)],
        compiler_params=_params(("arbitrary",)))(x, g.reshape(1, d), target)


def _shift_down(x, s):
    rows = lax.broadcasted_iota(jnp.int32, x.shape, 0)
    return jnp.where(rows >= s, pltpu.roll(x, s, 0), 0.0)


def _shift_up(x, s):
    n = x.shape[0]
    rows = lax.broadcasted_iota(jnp.int32, x.shape, 0)
    return jnp.where(rows < n - s, pltpu.roll(x, n - s, 0), 0.0)


def _sigmoid(x):
    return 1.0 / (1.0 + jnp.exp(-x))


def _silu_and_grad(x):
    s = _sigmoid(x)
    return x * s, s * (1.0 + x * (1.0 - s))


_GELU_C0, _GELU_C1 = math.sqrt(2.0 / math.pi), 0.044715


def _gelu_and_grad(x):
    th = jnp.tanh(_GELU_C0 * (x + _GELU_C1 * x * x * x))
    y = 0.5 * x * (1.0 + th)
    dy = 0.5 * (1.0 + th) + 0.5 * x * (1.0 - th * th) * _GELU_C0 * (1.0 + 3.0 * _GELU_C1 * x * x)
    return y, dy


def _ffn_act_fwd(h, w, name, tc=256, comm=None):
    t = h.shape[0]
    nb = D_FF // tc

    def body(hg_ref, hv_ref, wg_ref, wv_ref, a_ref):
        def conv(x, wr):
            return wr[2:3, :] * x + wr[1:2, :] * _shift_down(x, 1) + wr[0:1, :] * _shift_down(x, 2)

        cg = conv(hg_ref[...], wg_ref[...])
        cv = conv(hv_ref[...], wv_ref[...])
        a_ref[...] = (cg * _sigmoid(cg) * cv).astype(BF16)

    return _call(
        body, name=name, grid=(nb,),
        in_specs=[pl.BlockSpec((t, tc), lambda j: (0, j)), pl.BlockSpec((t, tc), lambda j: (0, j + nb)),
                  pl.BlockSpec((CONV_FFN, tc), lambda j: (0, j)), pl.BlockSpec((CONV_FFN, tc), lambda j: (0, j + nb))],
        out_specs=pl.BlockSpec((t, tc), lambda j: (0, j)), out_shape=jax.ShapeDtypeStruct((t, D_FF), BF16),
        sem=("parallel",), args=(h, h, w, w), comm=comm)


def _ffn_act_bwd(h, w, da, name, tc=256, comm=None):
    t = h.shape[0]
    nb = D_FF // tc

    def body(hg_ref, hv_ref, wg_ref, wv_ref, da_ref, dhg_ref, dhv_ref, dwg_ref, dwv_ref):
        hg, hv, wg, wv = hg_ref[...], hv_ref[...], wg_ref[...], wv_ref[...]
        hg1, hg2, hv1, hv2 = _shift_down(hg, 1), _shift_down(hg, 2), _shift_down(hv, 1), _shift_down(hv, 2)
        cg = wg[2:3, :] * hg + wg[1:2, :] * hg1 + wg[0:1, :] * hg2
        cv = wv[2:3, :] * hv + wv[1:2, :] * hv1 + wv[0:1, :] * hv2
        sg, dsg = _silu_and_grad(cg)
        dav = da_ref[...].astype(F32)
        dcv = dav * sg
        dcg = dav * cv * dsg

        def conv_t(dc, wr):
            return wr[2:3, :] * dc + wr[1:2, :] * _shift_up(dc, 1) + wr[0:1, :] * _shift_up(dc, 2)

        dhg_ref[...] = conv_t(dcg, wg).astype(BF16)
        dhv_ref[...] = conv_t(dcv, wv).astype(BF16)
        dwg_ref[0:1, :] = jnp.sum(dcg * hg2, axis=0, keepdims=True)
        dwg_ref[1:2, :] = jnp.sum(dcg * hg1, axis=0, keepdims=True)
        dwg_ref[2:3, :] = jnp.sum(dcg * hg, axis=0, keepdims=True)
        dwv_ref[0:1, :] = jnp.sum(dcv * hv2, axis=0, keepdims=True)
        dwv_ref[1:2, :] = jnp.sum(dcv * hv1, axis=0, keepdims=True)
        dwv_ref[2:3, :] = jnp.sum(dcv * hv, axis=0, keepdims=True)

    big = lambda off: pl.BlockSpec((t, tc), lambda j: (0, j + off))
    small = lambda off: pl.BlockSpec((CONV_FFN, tc), lambda j: (0, j + off))
    res = _call(
        body, name=name, grid=(nb,),
        in_specs=[big(0), big(nb), small(0), small(nb), big(0)],
        out_specs=[big(0), big(0), small(0), small(0)],
        out_shape=[jax.ShapeDtypeStruct((t, D_FF), BF16), jax.ShapeDtypeStruct((t, D_FF), BF16),
                   jax.ShapeDtypeStruct((CONV_FFN, D_FF), F32), jax.ShapeDtypeStruct((CONV_FFN, D_FF), F32)],
        sem=("parallel",), args=(h, h, w, w, da), comm=comm)
    (dhg, dhv, dwg, dwv), couts = res if comm is not None else (res, None)
    out = (jnp.concatenate([dhg, dhv], axis=1), jnp.concatenate([dwg, dwv], axis=1))
    return out if comm is None else (out, couts)


def _attn_probs(q, k):
    s = lax.dot_general(q.astype(BF16), k.astype(BF16), (((1,), (1,)), ((), ())),
                        preferred_element_type=F32) * (HEAD_X ** -0.5)
    s = s - jnp.max(s, axis=-1, keepdims=True)
    p = jnp.exp(s)
    return p / jnp.sum(p, axis=-1, keepdims=True)


def _attn_fwd(q, kv, name, tq=512, comm=None):
    t = q.shape[0]

    def body(q_ref, k_ref, v_ref, o_ref):
        p = _attn_probs(q_ref[...], k_ref[...])
        o_ref[...] = jnp.dot(p.astype(BF16), v_ref[...].astype(BF16), preferred_element_type=F32).astype(BF16)

    return _call(
        body, name=name, grid=(N_HEADS_X, t // tq),
        in_specs=[pl.BlockSpec((tq, HEAD_X), lambda h, i: (i, h)),
                  pl.BlockSpec((MEM_LEN, HEAD_X), lambda h, i: (0, h)),
                  pl.BlockSpec((MEM_LEN, HEAD_X), lambda h, i: (0, h + N_HEADS_X))],
        out_specs=pl.BlockSpec((tq, HEAD_X), lambda h, i: (i, h)),
        out_shape=jax.ShapeDtypeStruct((t, N_HEADS_X * HEAD_X), BF16),
        sem=("parallel", "parallel"), args=(q, kv, kv), comm=comm)


def _attn_bwd(q, kv, do, name, tq=512):
    t = q.shape[0]

    def body(q_ref, k_ref, v_ref, do_ref, dq_ref, dk_ref, dv_ref):
        @pl.when(pl.program_id(1) == 0)
        def _():
            dk_ref[...] = jnp.zeros_like(dk_ref)
            dv_ref[...] = jnp.zeros_like(dv_ref)

        qb, kb, vb, dob = (r[...].astype(BF16) for r in (q_ref, k_ref, v_ref, do_ref))
        p = _attn_probs(qb, kb)
        dp = lax.dot_general(dob, vb, (((1,), (1,)), ((), ())), preferred_element_type=F32)
        ds = p * (dp - jnp.sum(dp * p, axis=-1, keepdims=True)) * (HEAD_X ** -0.5)
        dsb = ds.astype(BF16)
        dq_ref[...] = jnp.dot(dsb, kb, preferred_element_type=F32).astype(BF16)
        dk_ref[...] += lax.dot_general(dsb, qb, (((0,), (0,)), ((), ())), preferred_element_type=F32)
        dv_ref[...] += lax.dot_general(p.astype(BF16), dob, (((0,), (0,)), ((), ())), preferred_element_type=F32)

    qs = pl.BlockSpec((tq, HEAD_X), lambda h, i: (i, h))
    ms = pl.BlockSpec((MEM_LEN, HEAD_X), lambda h, i: (0, h))
    return pl.pallas_call(
        body, name=name, grid=(N_HEADS_X, t // tq),
        in_specs=[qs, ms, pl.BlockSpec((MEM_LEN, HEAD_X), lambda h, i: (0, h + N_HEADS_X)), qs],
        out_specs=[qs, ms, ms],
        out_shape=[jax.ShapeDtypeStruct((t, D_MODEL), BF16), jax.ShapeDtypeStruct((MEM_LEN, D_MODEL), F32),
                   jax.ShapeDtypeStruct((MEM_LEN, D_MODEL), F32)],
        compiler_params=_params(("parallel", "arbitrary")))(q, kv, kv, do)


def _pool_counts(t, win):
    pos = lax.broadcasted_iota(jnp.int32, (t, 1), 0).astype(F32) + 1.0
    return 1.0 / jnp.minimum(pos, float(win))


def _pool_delta(xv, win):
    s, step = xv, 1
    while step < win:
        s = s + _shift_down(s, step)
        step *= 2
    return s * _pool_counts(xv.shape[0], win) - xv


def _pool_delta_t(dv, win):
    s, step = dv * _pool_counts(dv.shape[0], win), 1
    while step < win:
        s = s + _shift_up(s, step)
        step *= 2
    return s - dv


def _pool_fwd(xn, w, scale, res, name):
    t = xn.shape[0]

    def make_branch(win, xn_ref, w_ref, s_ref, r_ref, o_ref):
        def branch():
            dl = _pool_delta(xn_ref[...], win)
            y = jnp.dot(dl.astype(BF16), w_ref[0], preferred_element_type=F32)
            o_ref[...] = r_ref[...] + y * s_ref[...]
        return branch

    def body(xn_ref, w_ref, s_ref, r_ref, o_ref):
        for gi, win in enumerate(POOL_WINDOWS):
            pl.when(pl.program_id(0) == gi)(make_branch(win, xn_ref, w_ref, s_ref, r_ref, o_ref))

    blk = pl.BlockSpec((t, POOL_GROUP), lambda g: (0, g))
    return pl.pallas_call(
        body, name=name, grid=(len(POOL_WINDOWS),),
        in_specs=[blk, pl.BlockSpec((1, POOL_GROUP, POOL_GROUP), lambda g: (g, 0, 0)),
                  pl.BlockSpec((1, POOL_GROUP), lambda g: (0, g)), blk],
        out_specs=blk, out_shape=jax.ShapeDtypeStruct((t, D_MODEL), F32),
        compiler_params=_params(("parallel",)))(xn, w, scale, res)


def _pool_bwd(xn, w, scale, dmix, name):
    t = xn.shape[0]

    def make_branch(win, xn_ref, w_ref, s_ref, d_ref, dxn_ref, dw_ref, ds_ref):
        def branch():
            dl = _pool_delta(xn_ref[...], win).astype(BF16)
            wv = w_ref[0]
            dm = d_ref[...]
            y = jnp.dot(dl, wv, preferred_element_type=F32)
            ds_ref[...] = jnp.sum(dm * y, axis=0, keepdims=True)
            dy = (dm * s_ref[...]).astype(BF16)
            dw_ref[0] = lax.dot_general(dl, dy, (((0,), (0,)), ((), ())), preferred_element_type=F32)
            ddl = lax.dot_general(dy, wv, (((1,), (1,)), ((), ())), preferred_element_type=F32)
            dxn_ref[...] = _pool_delta_t(ddl, win)
        return branch

    def body(*refs):
        for gi, win in enumerate(POOL_WINDOWS):
            pl.when(pl.program_id(0) == gi)(make_branch(win, *refs))

    blk = pl.BlockSpec((t, POOL_GROUP), lambda g: (0, g))
    wspec = pl.BlockSpec((1, POOL_GROUP, POOL_GROUP), lambda g: (g, 0, 0))
    vec = pl.BlockSpec((1, POOL_GROUP), lambda g: (0, g))
    return pl.pallas_call(
        body, name=name, grid=(len(POOL_WINDOWS),), in_specs=[blk, wspec, vec, blk], out_specs=[blk, wspec, vec],
        out_shape=[jax.ShapeDtypeStruct((t, D_MODEL), F32),
                   jax.ShapeDtypeStruct((len(POOL_WINDOWS), POOL_GROUP, POOL_GROUP), F32),
                   jax.ShapeDtypeStruct((1, D_MODEL), F32)],
        compiler_params=_params(("parallel",)))(xn, w, scale, dmix)


def _qkv_conv(h, wr):
    return (wr[3:4, :] * h + wr[2:3, :] * _shift_down(h, 1) + wr[1:2, :] * _shift_down(h, 2)
            + wr[0:1, :] * _shift_down(h, 3))


def _qkv_block_kind(j):
    return j < 2 * N_HEADS_A, jnp.where(j < N_HEADS_A, HEAD_A ** -0.5, 1.0)


def _qkv_pre_fwd(h, w, name):
    t, cols = h.shape

    def body(h_ref, w_ref, o_ref):
        normalised, scale = _qkv_block_kind(pl.program_id(0))
        c = _qkv_conv(h_ref[...], w_ref[...])
        s = c * _sigmoid(c)
        r = lax.rsqrt(jnp.sum(s * s, axis=-1, keepdims=True) + 1e-6)
        o_ref[...] = jnp.where(normalised, s * (r * scale), s)

    blk = pl.BlockSpec((t, HEAD_A), lambda j: (0, j))
    return pl.pallas_call(
        body, name=name, grid=(cols // HEAD_A,), in_specs=[blk, pl.BlockSpec((CONV_A, HEAD_A), lambda j: (0, j))],
        out_specs=blk, out_shape=jax.ShapeDtypeStruct((t, cols), F32), compiler_params=_params(("parallel",)))(h, w)


def _qkv_pre_bwd(h, w, dy, name):
    t, cols = h.shape

    def body(h_ref, w_ref, dy_ref, dh_ref, dw_ref):
        normalised, scale = _qkv_block_kind(pl.program_id(0))
        hv, wr, dyv = h_ref[...], w_ref[...], dy_ref[...]
        h1, h2, h3 = _shift_down(hv, 1), _shift_down(hv, 2), _shift_down(hv, 3)
        c = wr[3:4, :] * hv + wr[2:3, :] * h1 + wr[1:2, :] * h2 + wr[0:1, :] * h3
        s, dsilu = _silu_and_grad(c)
        r = lax.rsqrt(jnp.sum(s * s, axis=-1, keepdims=True) + 1e-6)
        y = s * r
        dys = dyv * scale
        ds = jnp.where(normalised, r * (dys - y * jnp.sum(dys * y, axis=-1, keepdims=True)), dyv)
        dc = ds * dsilu
        dh = (wr[3:4, :] * dc + wr[2:3, :] * _shift_up(dc, 1) + wr[1:2, :] * _shift_up(dc, 2)
              + wr[0:1, :] * _shift_up(dc, 3))
        dh_ref[...] = dh.astype(BF16)
        dw_ref[0:1, :] = jnp.sum(dc * h3, axis=0, keepdims=True)
        dw_ref[1:2, :] = jnp.sum(dc * h2, axis=0, keepdims=True)
        dw_ref[2:3, :] = jnp.sum(dc * h1, axis=0, keepdims=True)
        dw_ref[3:4, :] = jnp.sum(dc * hv, axis=0, keepdims=True)

    blk = pl.BlockSpec((t, HEAD_A), lambda j: (0, j))
    taps = pl.BlockSpec((CONV_A, HEAD_A), lambda j: (0, j))
    return pl.pallas_call(
        body, name=name, grid=(cols // HEAD_A,), in_specs=[blk, taps, blk], out_specs=[blk, taps],
        out_shape=[jax.ShapeDtypeStruct((t, cols), BF16), jax.ShapeDtypeStruct((CONV_A, cols), F32)],
        compiler_params=_params(("parallel",)))(h, w, dy)


def _softplus(x):
    return jnp.maximum(x, 0.0) + jnp.log1p(jnp.exp(-jnp.abs(x)))


def _gates_fwd(ba, arow, brow, name):
    t = ba.shape[0]

    def body(x_ref, a_ref, b_ref, o_ref):
        xv = x_ref[...]
        lane = lax.broadcasted_iota(jnp.int32, xv.shape, 1)
        beta = _sigmoid(xv)
        g = -jnp.exp(a_ref[...]) * _softplus(xv + b_ref[...])
        o_ref[...] = jnp.where(lane < N_HEADS_A, beta, jnp.where(lane < 2 * N_HEADS_A, g, 0.0))

    return pl.pallas_call(body, name=name, out_shape=jax.ShapeDtypeStruct((t, LANE), F32),
                          compiler_params=_params())(ba, arow, brow)


def _gates_bwd(ba, arow, brow, dgb, name):
    t = ba.shape[0]

    def body(x_ref, a_ref, b_ref, d_ref, dx_ref, da_ref, db_ref):
        xv = x_ref[...]
        dv = d_ref[0] + d_ref[1] + d_ref[2] + d_ref[3]
        lane = lax.broadcasted_iota(jnp.int32, xv.shape, 1)
        beta = _sigmoid(xv)
        ea = jnp.exp(a_ref[...])
        z = xv + b_ref[...]
        dgv = jnp.where((lane >= N_HEADS_A) & (lane < 2 * N_HEADS_A), dv, 0.0) * (-ea)
        dz = dgv * _sigmoid(z)
        dx = jnp.where(lane < N_HEADS_A, dv * beta * (1.0 - beta), dz)
        dx_ref[...] = dx.astype(BF16)
        db_ref[...] = jnp.sum(dz, axis=0, keepdims=True)
        da_ref[...] = jnp.sum(dgv * _softplus(z), axis=0, keepdims=True)

    return pl.pallas_call(
        body, name=name,
        out_shape=[jax.ShapeDtypeStruct((t, LANE), BF16), jax.ShapeDtypeStruct((1, LANE), F32),
                   jax.ShapeDtypeStruct((1, LANE), F32)],
        compiler_params=_params())(ba, arow, brow, dgb)


def _head_gates(gates, head):
    lane = lax.broadcasted_iota(jnp.int32, gates.shape, 1)
    beta = jnp.sum(jnp.where(lane == head, gates, 0.0), axis=1, keepdims=True)
    g = jnp.sum(jnp.where(lane == head + N_HEADS_A, gates, 0.0), axis=1, keepdims=True)
    return beta, g


_B_NN, _B_NT, _B_TN = ((2,), (1,)), ((2,), (2,)), ((1,), (1,))


def _bdot(a, b, dims=_B_NN, prec=None):
    if prec is None:
        a, b = a.astype(BF16), b.astype(BF16)
    return lax.dot_general(a, b, (dims, ((0,), (0,))), precision=prec, preferred_element_type=F32)


def _heads_of(ref):
    return jnp.stack([ref[:, h * HEAD_A:(h + 1) * HEAD_A] for h in range(N_HEADS_A)])


def _all_head_gates(gates):
    pairs = [_head_gates(gates, h) for h in range(N_HEADS_A)]
    return jnp.stack([b for b, _ in pairs]), jnp.stack([g for _, g in pairs])


def _gdr_terms(k, beta, g):
    h, c = k.shape[0], GDR_CHUNK
    row = lax.broadcasted_iota(jnp.int32, (c, c), 0)
    col = lax.broadcasted_iota(jnp.int32, (c, c), 1)
    causal, strict = row >= col, row > col
    lower = jnp.broadcast_to(causal.astype(F32), (h, c, c))
    gcum = _bdot(lower, jnp.broadcast_to(g, (h, c, c)), prec=HIGHEST)
    diff = gcum - jnp.swapaxes(gcum, 1, 2)
    decay = jnp.where(causal, jnp.exp(jnp.where(causal, diff, 0.0)), 0.0)
    kb = k * beta
    return row, col, causal, strict, gcum, decay, kb, _bdot(kb, k, _B_NT)


def _unit_lower_inverses(a):
    c = a.shape[1]
    eye = (lax.broadcasted_iota(jnp.int32, (c, c), 0) == lax.broadcasted_iota(jnp.int32, (c, c), 1)).astype(F32)
    p = -a
    inv = eye + p
    step = 1
    while 2 * step < c:
        p = _bdot(p, p, prec=HIGH)
        inv = inv + _bdot(inv, p, prec=HIGH)
        step *= 2
    return inv


def _gdr_fwd(qkv, gates, name, comm=None):
    t = qkv.shape[0]
    c, nh = GDR_CHUNK, N_HEADS_A
    n = t // c

    def body(q_ref, k_ref, v_ref, gb_ref, o_ref, tm_ref, s_ref, state):
        @pl.when(pl.program_id(0) == 0)
        def _():
            state[...] = jnp.zeros_like(state)

        qv, kv, vv = _heads_of(q_ref), _heads_of(k_ref), _heads_of(v_ref)
        beta, g = _all_head_gates(gb_ref[...])
        row, col, causal, strict, gcum, decay, kb, kk = _gdr_terms(kv, beta, g)
        tm = _unit_lower_inverses(jnp.where(strict, kk * decay, 0.0))
        e = jnp.exp(gcum)
        u = _bdot(tm, vv * beta, prec=HIGH)
        w = _bdot(tm, kb * e, prec=HIGH)
        p = jnp.where(causal, _bdot(qv, kv, _B_NT) * decay, 0.0)
        s = state[...]
        s_ref[:, 0] = s
        tm_ref[:, 0] = tm
        vn = u - _bdot(w, s)
        o = _bdot(qv * e, s) + _bdot(p, vn)
        for h in range(nh):
            o_ref[:, h * HEAD_A:(h + 1) * HEAD_A] = o[h]
        glast = gcum[:, c - 1:c, :]
        state[...] = s * jnp.exp(glast) + _bdot(kv * jnp.exp(glast - gcum), vn, _B_TN)

    part = lambda p: pl.BlockSpec((c, WIDTH_A), lambda i: (i, p))
    mat = pl.BlockSpec((nh, 1, c, c), lambda i: (0, i, 0, 0))
    return _call(
        body, name=name, grid=(n,), in_specs=[part(0), part(1), part(2), pl.BlockSpec((c, LANE), lambda i: (i, 0))],
        out_specs=[part(0), mat, mat],
        out_shape=[jax.ShapeDtypeStruct((t, WIDTH_A), F32), jax.ShapeDtypeStruct((nh, n, c, c), F32),
                   jax.ShapeDtypeStruct((nh, n, HEAD_A, HEAD_A), F32)],
        scratch_shapes=[pltpu.VMEM((nh, HEAD_A, HEAD_A), F32)], sem=("arbitrary",),
        args=(qkv, qkv, qkv, gates), comm=comm)


def _gdr_bwd(qkv, gates, tm_all, s_all, do, name, comm=None):
    t = qkv.shape[0]
    c, nh = GDR_CHUNK, N_HEADS_A
    n = t // c

    def body(q_ref, k_ref, v_ref, gb_ref, tm_ref, s_ref, do_ref, dqkv_ref, dgb_ref, dstate):
        @pl.when(pl.program_id(0) == 0)
        def _():
            dstate[...] = jnp.zeros_like(dstate)

        qv, kv, vv, dov = _heads_of(q_ref), _heads_of(k_ref), _heads_of(v_ref), _heads_of(do_ref)
        beta, g = _all_head_gates(gb_ref[...])
        tm, s, dsp = tm_ref[:, 0], s_ref[:, 0], dstate[...]
        row, col, causal, strict, gcum, decay, kb, kk = _gdr_terms(kv, beta, g)
        rowsum = lambda x: jnp.sum(x, axis=2, keepdims=True)
        e = jnp.exp(gcum)
        vb, kbe = vv * beta, kb * e
        u = _bdot(tm, vb, prec=HIGH)
        w = _bdot(tm, kbe, prec=HIGH)
        qk = _bdot(qv, kv, _B_NT)
        p = jnp.where(causal, qk * decay, 0.0)
        vn = u - _bdot(w, s)
        glast = gcum[:, c - 1:c, :]
        el = jnp.exp(glast)
        f = jnp.exp(glast - gcum)
        kd = kv * f
        qe = qv * e

        dvn = _bdot(p, dov, _B_TN) + _bdot(kd, dsp)
        dglast = el[:, :, 0:1] * jnp.sum(s * dsp, axis=(1, 2), keepdims=True)
        dkd = _bdot(vn, dsp, _B_NT)
        dk = dkd * f
        df = rowsum(dkd * kv) * f[:, :, 0:1]
        dglast = dglast + jnp.sum(df, axis=1, keepdims=True)
        dgc = -df
        dp = jnp.where(causal, _bdot(dov, vn, _B_NT), 0.0)
        dqe = _bdot(dov, s, _B_NT)
        dq = dqe * e
        de = rowsum(dqe * qv)
        dstate[...] = dsp * el + _bdot(qe, dov, _B_TN) - _bdot(w, dvn, _B_TN)
        dw = -_bdot(dvn, s, _B_NT)
        dvb = _bdot(tm, dvn, _B_TN, prec=HIGH)
        dkbe = _bdot(tm, dw, _B_TN, prec=HIGH)
        da = -jnp.where(strict, _bdot(dvb, u, _B_NT) + _bdot(dkbe, w, _B_NT), 0.0)
        dkk = da * decay
        dqk = dp * decay
        dd = da * kk + dp * qk
        dq = dq + _bdot(dqk, kv)
        dk = dk + _bdot(dqk, qv, _B_TN)
        dkb = _bdot(dkk, kv) + dkbe * e
        dk = dk + _bdot(dkk, kb, _B_TN)
        de = de + rowsum(dkbe * kb)
        dk = dk + dkb * beta
        dbeta = rowsum(dkb * kv) + rowsum(dvb * vv)
        m = dd * decay
        dgc = dgc + rowsum(m) - rowsum(jnp.swapaxes(m, 1, 2))
        dgc = dgc + de * e[:, :, 0:1]
        dgc = dgc + jnp.where(row[:, 0:1] == c - 1, dglast, 0.0)
        upper = jnp.broadcast_to((row <= col).astype(F32), (nh, c, c))
        dg = _bdot(upper, jnp.broadcast_to(dgc, (nh, c, c)), prec=HIGHEST)
        dv = dvb * beta
        for p, grad in enumerate((dq, dk, dv)):
            for h in range(nh):
                dqkv_ref[:, p * WIDTH_A + h * HEAD_A:p * WIDTH_A + (h + 1) * HEAD_A] = grad[h]
        head = lax.broadcasted_iota(jnp.int32, (nh, c, LANE), 0)
        lane = lax.broadcasted_iota(jnp.int32, (nh, c, LANE), 2)
        dgb_ref[...] = jnp.where(lane == head, dbeta, jnp.where(lane == head + nh, dg, 0.0))

    part = lambda p: pl.BlockSpec((c, WIDTH_A), lambda i: (n - 1 - i, p))
    mat = pl.BlockSpec((nh, 1, c, c), lambda i: (0, n - 1 - i, 0, 0))
    return _call(
        body, name=name, grid=(n,),
        in_specs=[part(0), part(1), part(2), pl.BlockSpec((c, LANE), lambda i: (n - 1 - i, 0)), mat, mat, part(0)],
        out_specs=[pl.BlockSpec((c, 3 * WIDTH_A), lambda i: (n - 1 - i, 0)),
                   pl.BlockSpec((nh, c, LANE), lambda i: (0, n - 1 - i, 0))],
        out_shape=[jax.ShapeDtypeStruct((t, 3 * WIDTH_A), F32), jax.ShapeDtypeStruct((nh, t, LANE), F32)],
        scratch_shapes=[pltpu.VMEM((nh, HEAD_A, HEAD_A), F32)], sem=("arbitrary",),
        args=(qkv, qkv, qkv, gates, tm_all, s_all, do), comm=comm)


def _onorm_fwd(o, gate, g, name):
    t = o.shape[0]

    def body(o_ref, gate_ref, g_ref, y_ref):
        ov, gv = o_ref[...], gate_ref[...]
        r = lax.rsqrt(jnp.mean(ov * ov, axis=-1, keepdims=True) + RMS_EPS)
        y_ref[...] = (ov * r * g_ref[...] * gv * _sigmoid(gv)).astype(BF16)

    blk = pl.BlockSpec((t, HEAD_A), lambda j: (0, j))
    return pl.pallas_call(
        body, name=name, grid=(N_HEADS_A,), in_specs=[blk, blk, pl.BlockSpec((1, HEAD_A), lambda j: (0, 0))],
        out_specs=blk, out_shape=jax.ShapeDtypeStruct((t, WIDTH_A), BF16),
        compiler_params=_params(("parallel",)))(o, gate, g)


def _onorm_bwd(o, gate, g, dy, name):
    t = o.shape[0]

    def body(o_ref, gate_ref, g_ref, dy_ref, do_ref, dgate_ref, dg_ref):
        @pl.when(pl.program_id(0) == 0)
        def _():
            dg_ref[...] = jnp.zeros_like(dg_ref)

        ov, gv, dyv = o_ref[...], gate_ref[...], dy_ref[...].astype(F32)
        r = lax.rsqrt(jnp.mean(ov * ov, axis=-1, keepdims=True) + RMS_EPS)
        oh = ov * r
        sg, dsg = _silu_and_grad(gv)
        dgate_ref[...] = (dyv * oh * g_ref[...] * dsg).astype(BF16)
        dn = dyv * sg
        dg_ref[...] += jnp.sum(dn * oh, axis=0, keepdims=True)
        dng = dn * g_ref[...]
        do_ref[...] = r * (dng - oh * jnp.mean(dng * oh, axis=-1, keepdims=True))

    blk = pl.BlockSpec((t, HEAD_A), lambda j: (0, j))
    vec = pl.BlockSpec((1, HEAD_A), lambda j: (0, 0))
    return pl.pallas_call(
        body, name=name, grid=(N_HEADS_A,), in_specs=[blk, blk, vec, blk], out_specs=[blk, blk, vec],
        out_shape=[jax.ShapeDtypeStruct((t, WIDTH_A), F32), jax.ShapeDtypeStruct((t, WIDTH_A), BF16),
                   jax.ShapeDtypeStruct((1, HEAD_A), F32)],
        compiler_params=_params(("arbitrary",)))(o, gate, g, dy)


def _cmul(ar, ai, br, bi):
    return ar * br - ai * bi, ar * bi + ai * br


def _scan_tables(ar, ai, reverse):
    p1 = (ar, ai)
    p2 = _cmul(*p1, *p1)
    p4 = _cmul(*p2, *p2)
    p8 = _cmul(*p4, *p4)
    p3 = _cmul(*p2, *p1)
    p5 = _cmul(*p4, *p1)
    p6 = _cmul(*p4, *p2)
    p7 = _cmul(*p4, *p3)
    pows = [p1, p2, p3, p4, p5, p6, p7, p8]
    rows = lax.broadcasted_iota(jnp.int32, (8, ar.shape[1]), 0)
    tr = jnp.zeros((8, ar.shape[1]), F32)
    ti = jnp.zeros((8, ar.shape[1]), F32)
    for r in range(8):
        pw = pows[7 - r] if reverse else pows[r]
        tr = jnp.where(rows == r, pw[0], tr)
        ti = jnp.where(rows == r, pw[1], ti)
    return p1, p2, p4, p8, tr, ti


def _tile_scan(xr, xi, p1, p2, p4, reverse):
    rows = lax.broadcasted_iota(jnp.int32, xr.shape, 0)
    for s, (pr, pi) in ((1, p1), (2, p2), (4, p4)):
        if reverse:
            keep = rows < 8 - s
            sr, si = pltpu.roll(xr, 8 - s, 0), pltpu.roll(xi, 8 - s, 0)
        else:
            keep = rows >= s
            sr, si = pltpu.roll(xr, s, 0), pltpu.roll(xi, s, 0)
        sr, si = jnp.where(keep, sr, 0.0), jnp.where(keep, si, 0.0)
        mr, mi = _cmul(pr, pi, sr, si)
        xr, xi = xr + mr, xi + mi
    return xr, xi


def _s5_scan_fwd(bu, a, name, tb=512, comm=None):
    t = bu.shape[0]
    cb = SCAN_CB
    nt = t // tb

    def body(b_ref, a_ref, x_ref, carry):
        @pl.when(pl.program_id(1) == 0)
        def _():
            carry[...] = jnp.zeros_like(carry)

        ar, ai = a_ref[:, 0:cb], a_ref[:, cb:2 * cb]
        p1, p2, p4, p8, tr, ti = _scan_tables(ar, ai, False)

        def step(j, c):
            cr, ci = c
            i = pl.multiple_of(j * 8, 8)
            xr, xi = _tile_scan(b_ref[pl.ds(i, 8), 0:cb], b_ref[pl.ds(i, 8), cb:2 * cb], p1, p2, p4, False)
            mr, mi = _cmul(tr, ti, cr, ci)
            xr, xi = xr + mr, xi + mi
            x_ref[pl.ds(i, 8), 0:cb] = xr
            x_ref[pl.ds(i, 8), cb:2 * cb] = xi
            return xr[7:8, :], xi[7:8, :]

        cr, ci = lax.fori_loop(0, tb // 8, step, (carry[0:1, :], carry[1:2, :]), unroll=2)
        carry[0:1, :] = cr
        carry[1:2, :] = ci

    blk = pl.BlockSpec((tb, 2 * cb), lambda j, i: (i, j))
    return _call(
        body, name=name, grid=(SSM_CH // cb, nt),
        in_specs=[blk, pl.BlockSpec((1, 2 * cb), lambda j, i: (0, j))], out_specs=blk,
        out_shape=jax.ShapeDtypeStruct((t, 2 * SSM_CH), F32), scratch_shapes=[pltpu.VMEM((8, cb), F32)],
        sem=("parallel", "arbitrary"), args=(bu, a), comm=comm)


def _s5_scan_bwd(dx, x, a, name, tb=512, comm=None):
    t = dx.shape[0]
    cb = SCAN_CB
    nt = t // tb
    nj = tb // 8

    def body(d_ref, x_ref, xp_ref, a_ref, l_ref, da_ref, carry, acc):
        tblk = pl.program_id(1)

        @pl.when(tblk == 0)
        def _():
            carry[...] = jnp.zeros_like(carry)
            acc[...] = jnp.zeros_like(acc)

        ar, ai = a_ref[:, 0:cb], a_ref[:, cb:2 * cb]
        p1, p2, p4, p8, tr, ti = _scan_tables(ar, -ai, True)
        rows = lax.broadcasted_iota(jnp.int32, (8, cb), 0)

        def step(jj, c):
            cr, ci, sr_acc, si_acc = c
            j = nj - 1 - jj
            i = pl.multiple_of(j * 8, 8)
            lr, li = _tile_scan(d_ref[pl.ds(i, 8), 0:cb], d_ref[pl.ds(i, 8), cb:2 * cb], p1, p2, p4, True)
            mr, mi = _cmul(tr, ti, cr, ci)
            lr, li = lr + mr, li + mi
            l_ref[pl.ds(i, 8), 0:cb] = lr
            l_ref[pl.ds(i, 8), cb:2 * cb] = li
            ip = pl.multiple_of(jnp.maximum(j - 1, 0) * 8, 8)
            prev_r = jnp.where(j > 0, x_ref[pl.ds(ip, 8), 0:cb], xp_ref[:, 0:cb])
            prev_i = jnp.where(j > 0, x_ref[pl.ds(ip, 8), cb:2 * cb], xp_ref[:, cb:2 * cb])
            edge = jnp.where(jnp.logical_and(j == 0, tblk == nt - 1), 0.0, 1.0)
            xs_r = jnp.where(rows == 0, pltpu.roll(prev_r, 1, 0) * edge, pltpu.roll(x_ref[pl.ds(i, 8), 0:cb], 1, 0))
            xs_i = jnp.where(rows == 0, pltpu.roll(prev_i, 1, 0) * edge, pltpu.roll(x_ref[pl.ds(i, 8), cb:2 * cb], 1, 0))
            sr_acc = sr_acc + lr * xs_r + li * xs_i
            si_acc = si_acc + li * xs_r - lr * xs_i
            return lr[0:1, :], li[0:1, :], sr_acc, si_acc

        cr, ci, sr_acc, si_acc = lax.fori_loop(
            0, nj, step, (carry[0:1, :], carry[1:2, :], acc[:, 0:cb], acc[:, cb:2 * cb]))
        carry[0:1, :] = cr
        carry[1:2, :] = ci
        acc[:, 0:cb] = sr_acc
        acc[:, cb:2 * cb] = si_acc

        @pl.when(tblk == nt - 1)
        def _():
            da_ref[...] = jnp.sum(acc[...], axis=0, keepdims=True)

    blk = pl.BlockSpec((tb, 2 * cb), lambda j, i: (nt - 1 - i, j))
    prev = pl.BlockSpec((8, 2 * cb), lambda j, i: (jnp.maximum((nt - 1 - i) * (tb // 8) - 1, 0), j))
    vec = pl.BlockSpec((1, 2 * cb), lambda j, i: (0, j))
    return _call(
        body, name=name, grid=(SSM_CH // cb, nt), in_specs=[blk, blk, prev, vec], out_specs=[blk, vec],
        out_shape=[jax.ShapeDtypeStruct((t, 2 * SSM_CH), F32), jax.ShapeDtypeStruct((1, 2 * SSM_CH), F32)],
        scratch_shapes=[pltpu.VMEM((8, cb), F32), pltpu.VMEM((8, 2 * cb), F32)],
        sem=("parallel", "arbitrary"), args=(dx, x, x, a), comm=comm)


def _glu_fwd(yc, u, dvec, wg, bg, name, tr=256):
    t = yc.shape[0]

    def body(yc_ref, u_ref, d_ref, w_ref, b_ref, yl_ref, yb_ref):
        yl = yc_ref[...] + d_ref[...] * u_ref[...]
        yl_ref[...] = yl
        yg, _ = _gelu_and_grad(yl)
        z = jnp.dot(yg.astype(BF16), w_ref[...], preferred_element_type=F32) + b_ref[...]
        yb_ref[...] = (yg * _sigmoid(z)).astype(BF16)

    blk = pl.BlockSpec((tr, SSM_WIDTH), lambda i: (i, 0))
    vec = pl.BlockSpec((1, SSM_WIDTH), lambda i: (0, 0))
    return pl.pallas_call(
        body, name=name, grid=(t // tr,),
        in_specs=[blk, blk, vec, pl.BlockSpec((SSM_WIDTH, SSM_WIDTH), lambda i: (0, 0)), vec],
        out_specs=[blk, blk],
        out_shape=[jax.ShapeDtypeStruct((t, SSM_WIDTH), F32), jax.ShapeDtypeStruct((t, SSM_WIDTH), BF16)],
        compiler_params=_params(("parallel",)))(yc, u, dvec, wg, bg)


def _glu_bwd(yl, u, dvec, wg, bg, dyb, name, tr=256):
    t = yl.shape[0]

    def body(yl_ref, u_ref, d_ref, w_ref, b_ref, dy_ref, dyl_ref, du_ref, dw_ref, db_ref, dd_ref):
        @pl.when(pl.program_id(0) == 0)
        def _():
            dw_ref[...] = jnp.zeros_like(dw_ref)
            db_ref[...] = jnp.zeros_like(db_ref)
            dd_ref[...] = jnp.zeros_like(dd_ref)

        ylv, dyv, wv = yl_ref[...], dy_ref[...].astype(F32), w_ref[...]
        yg, dgelu = _gelu_and_grad(ylv)
        ygb = yg.astype(BF16)
        z = jnp.dot(ygb, wv, preferred_element_type=F32) + b_ref[...]
        sg = _sigmoid(z)
        dz = dyv * yg * sg * (1.0 - sg)
        dzb = dz.astype(BF16)
        dyg = dyv * sg + lax.dot_general(dzb, wv, (((1,), (1,)), ((), ())), preferred_element_type=F32)
        dyl = dyg * dgelu
        dyl_ref[...] = dyl.astype(BF16)
        du_ref[...] = dyl * d_ref[...]
        dw_ref[...] += lax.dot_general(ygb, dzb, (((0,), (0,)), ((), ())), preferred_element_type=F32)
        db_ref[...] += jnp.sum(dz, axis=0, keepdims=True)
        dd_ref[...] += jnp.sum(dyl * u_ref[...], axis=0, keepdims=True)

    blk = pl.BlockSpec((tr, SSM_WIDTH), lambda i: (i, 0))
    vec = pl.BlockSpec((1, SSM_WIDTH), lambda i: (0, 0))
    wsp = pl.BlockSpec((SSM_WIDTH, SSM_WIDTH), lambda i: (0, 0))
    return pl.pallas_call(
        body, name=name, grid=(t // tr,), in_specs=[blk, blk, vec, wsp, vec, blk],
        out_specs=[blk, blk, wsp, vec, vec],
        out_shape=[jax.ShapeDtypeStruct((t, SSM_WIDTH), BF16), jax.ShapeDtypeStruct((t, SSM_WIDTH), F32),
                   jax.ShapeDtypeStruct((SSM_WIDTH, SSM_WIDTH), F32), jax.ShapeDtypeStruct((1, SSM_WIDTH), F32),
                   jax.ShapeDtypeStruct((1, SSM_WIDTH), F32)],
        compiler_params=_params(("arbitrary",)))(yl, u, dvec, wg, bg, dyb)


def _mesh_pos():
    return lax.axis_index("x"), lax.axis_index("y"), lax.axis_index("c")


def _device_index():
    x, y, c = _mesh_pos()
    return 4 * x + 2 * y + c


def _gather_comm(arrays):
    na = len(arrays)

    def own_copy(ins, outs, sems, ai):
        return pltpu.make_async_copy(ins[ai], outs[ai].at[_device_index()], sems[2].at[ai])

    def ctx(ins, outs, sems):
        send_sems, recv_sems = sems[:2]
        x, y, c = _mesh_pos()
        chips = [(1 - x, y), (x, 1 - y), (1 - x, 1 - y)]

        def copy(ai, kk, block, to, own=False):
            slot = outs[ai].at[4 * block[0] + 2 * block[1] + block[2]]
            return pltpu.make_async_remote_copy(
                src_ref=ins[ai] if own else slot, dst_ref=slot, send_sem=send_sems.at[ai, kk],
                recv_sem=recv_sems.at[ai, kk], device_id=to, device_id_type=MESH)

        return (x, y, c), (x, y, 1 - c), chips, c, copy

    def start(ins, outs, sems):
        me, sibling, chips, c, copy = ctx(ins, outs, sems)
        for ai in range(na):
            copy(ai, 0, me, sibling, own=True).start()
            for j, chip in enumerate(chips):
                copy(ai, 1 + j, me, (*chip, c), own=True).start()
        for ai in range(na):
            own_copy(ins, outs, sems, ai).start()

    def mid(ins, outs, sems):
        me, sibling, chips, c, copy = ctx(ins, outs, sems)
        for ai in range(na):
            for j, chip in enumerate(chips):
                copy(ai, 1 + j, (*chip, c), me).wait_recv()
                copy(ai, 4 + j, (*chip, c), sibling).start()

    def end(ins, outs, sems):
        me, sibling, chips, c, copy = ctx(ins, outs, sems)
        for ai in range(na):
            copy(ai, 0, sibling, me).wait_recv()
            copy(ai, 0, me, sibling, own=True).wait_send()
            for j, chip in enumerate(chips):
                copy(ai, 4 + j, (*chip, 1 - c), me).wait_recv()
                copy(ai, 1 + j, me, (*chip, c), own=True).wait_send()
                copy(ai, 4 + j, (*chip, c), sibling).wait_send()
            own_copy(ins, outs, sems, ai).wait()

    return Comm(arrays, [jax.ShapeDtypeStruct((N_DEV,) + a.shape, a.dtype) for a in arrays],
                [pltpu.SemaphoreType.DMA((na, 7)), pltpu.SemaphoreType.DMA((na, 7)), pltpu.SemaphoreType.DMA((na,))],
                start, end, mid)


def _sequencer_gather(arrays, name, collective_id):
    comm = _gather_comm(arrays)
    na = len(arrays)

    def body(*refs):
        ins, outs, sems = refs[:na], refs[na:2 * na], refs[2 * na:]
        x, y, c = _mesh_pos()
        peers = [(x, y, 1 - c), (1 - x, y, c), (x, 1 - y, c), (1 - x, 1 - y, c)]
        barrier = pltpu.get_barrier_semaphore()
        for peer in peers:
            pl.semaphore_signal(barrier, inc=1, device_id=peer, device_id_type=MESH)
        pl.semaphore_wait(barrier, len(peers))
        comm.start(ins, outs, sems)
        comm.mid(ins, outs, sems)
        comm.end(ins, outs, sems)

    return list(pl.kernel(
        body, out_type=tuple(comm.out_shapes), mesh=plsc.ScalarSubcoreMesh(axis_name="sequencer", num_cores=1),
        name=name, scratch_types=tuple(comm.sems),
        compiler_params=pltpu.CompilerParams(collective_id=collective_id))(*arrays))


def _sequencer_exchange(comm, peers_of, name, collective_id):
    na = len(comm.inputs)

    def body(*refs):
        ins, outs, sems = refs[:na], refs[na:na + len(comm.out_shapes)], refs[na + len(comm.out_shapes):]
        peers = peers_of(*_mesh_pos())
        barrier = pltpu.get_barrier_semaphore()
        for peer in peers:
            pl.semaphore_signal(barrier, inc=1, device_id=peer, device_id_type=MESH)
        pl.semaphore_wait(barrier, len(peers))
        comm.start(ins, outs, sems)
        comm.end(ins, outs, sems)

    return list(pl.kernel(
        body, out_type=tuple(comm.out_shapes), mesh=plsc.ScalarSubcoreMesh(axis_name="sequencer", num_cores=1),
        name=name, scratch_types=tuple(comm.sems),
        compiler_params=pltpu.CompilerParams(collective_id=collective_id))(*comm.inputs))


SIBLING_SWAP_ID, CHIP_EXCHANGE_ID = 9, 10


def _sequencer_swap(arrays, name):
    return _sequencer_exchange(_swap_comm(arrays), lambda x, y, c: [(x, y, 1 - c)], name, SIBLING_SWAP_ID)[0]


def _sequencer_chips(send, name):
    return _sequencer_exchange(_chips_comm(send), lambda x, y, c: [(1 - x, y, c), (x, 1 - y, c), (1 - x, 1 - y, c)],
                               name, CHIP_EXCHANGE_ID)[0]


def _swap_comm(arrays):
    na = len(arrays)
    offs = np.concatenate([[0], np.cumsum([a.shape[1] for a in arrays])]).astype(int)

    def copies(ins, outs, sems):
        x, y, c = _mesh_pos()
        return [pltpu.make_async_remote_copy(
            src_ref=ins[ai].at[2 * k + 1 - c], dst_ref=outs[0].at[k, pl.ds(int(offs[ai]), arrays[ai].shape[1])],
            send_sem=sems[0].at[ai, k], recv_sem=sems[1].at[ai, k], device_id=(x, y, 1 - c), device_id_type=MESH)
            for ai in range(na) for k in range(4)]

    def start(ins, outs, sems):
        for cp in copies(ins, outs, sems):
            cp.start()

    def end(ins, outs, sems):
        for cp in copies(ins, outs, sems):
            cp.wait()

    return Comm(arrays, [jax.ShapeDtypeStruct((4, int(offs[-1]), PACK_COLS), arrays[0].dtype)],
                [pltpu.SemaphoreType.DMA((na, 4)), pltpu.SemaphoreType.DMA((na, 4))], start, end)


def _chips_comm(send):
    def copies(ins, outs, sems):
        x, y, c = _mesh_pos()
        chips = [(1 - x, y), (x, 1 - y), (1 - x, 1 - y)]
        return [pltpu.make_async_remote_copy(
            src_ref=ins[0].at[2 * cx + cy], dst_ref=outs[0].at[j], send_sem=sems[0].at[j], recv_sem=sems[1].at[j],
            device_id=(cx, cy, c), device_id_type=MESH) for j, (cx, cy) in enumerate(chips)]

    def start(ins, outs, sems):
        for cp in copies(ins, outs, sems):
            cp.start()

    def end(ins, outs, sems):
        for cp in copies(ins, outs, sems):
            cp.wait()

    return Comm([send], [jax.ShapeDtypeStruct((3,) + send.shape[1:], send.dtype)],
                [pltpu.SemaphoreType.DMA((3,)), pltpu.SemaphoreType.DMA((3,))], start, end)


def _pair_sum(keep, recv, name, tr=464):
    nchip, rows, cols = keep.shape

    def body(g_ref, r_ref, o_ref):
        o_ref[...] = (g_ref[...].astype(F32) + r_ref[...].astype(F32)).astype(BF16)

    blk = pl.BlockSpec((1, tr, cols), lambda k, i: (k, i, 0))
    return pl.pallas_call(
        body, name=name, grid=(nchip, rows // tr), in_specs=[blk, blk], out_specs=blk,
        out_shape=jax.ShapeDtypeStruct((nchip, rows, cols), BF16),
        compiler_params=_params(("parallel", "parallel")))(keep, recv)


def _pair_sum_pieces(pieces, recv, name, tr):
    _, rows, cols = pieces.shape
    core = lax.axis_index("c").astype(jnp.int32).reshape(1)

    def body(c_ref, g_ref, r_ref, o_ref):
        del c_ref
        o_ref[...] = (g_ref[...].astype(F32) + r_ref[...].astype(F32)).astype(BF16)

    grid_spec = pltpu.PrefetchScalarGridSpec(
        num_scalar_prefetch=1, grid=(4, rows // tr),
        in_specs=[pl.BlockSpec((1, tr, cols), lambda k, i, c_ref: (2 * k + c_ref[0], i, 0)),
                  pl.BlockSpec((1, tr, cols), lambda k, i, c_ref: (k, i, 0))],
        out_specs=pl.BlockSpec((1, tr, cols), lambda k, i, c_ref: (k, i, 0)))
    return pl.pallas_call(
        body, name=name, grid_spec=grid_spec, out_shape=jax.ShapeDtypeStruct((4, rows, cols), BF16),
        compiler_params=_params(("parallel", "parallel")))(core, pieces, recv)


def _chip_sum(own, others, name, tr=464):
    _, rows, cols = own.shape
    chip = (2 * lax.axis_index("x") + lax.axis_index("y")).astype(jnp.int32).reshape(1)

    def body(chip_ref, own_ref, oth_ref, o_ref):
        del chip_ref
        acc = own_ref[0].astype(F32)
        for j in range(3):
            acc = acc + oth_ref[j].astype(F32)
        o_ref[...] = acc

    grid_spec = pltpu.PrefetchScalarGridSpec(
        num_scalar_prefetch=1, grid=(rows // tr,),
        in_specs=[pl.BlockSpec((1, tr, cols), lambda i, chip_ref: (chip_ref[0], i, 0)),
                  pl.BlockSpec((3, tr, cols), lambda i, chip_ref: (0, i, 0))],
        out_specs=pl.BlockSpec((tr, cols), lambda i, chip_ref: (i, 0)))
    return pl.pallas_call(
        body, name=name, grid_spec=grid_spec, out_shape=jax.ShapeDtypeStruct((rows, cols), F32),
        compiler_params=_params(("parallel",)))(chip, own, others)


def _sum_leading(parts, name, tr=464):
    nparts, rows, cols = parts.shape
    tr = tr if rows % tr == 0 else rows

    def body(p_ref, o_ref):
        acc = p_ref[0].astype(F32)
        for i in range(1, nparts):
            acc = acc + p_ref[i].astype(F32)
        o_ref[...] = acc

    return pl.pallas_call(
        body, name=name, grid=(rows // tr,),
        in_specs=[pl.BlockSpec((nparts, tr, cols), lambda i: (0, i, 0))],
        out_specs=pl.BlockSpec((tr, cols), lambda i: (i, 0)), out_shape=jax.ShapeDtypeStruct((rows, cols), F32),
        compiler_params=_params(("parallel",)))(parts)


def _adamw(w, g, m, v, name, comm=None):
    shape = w.shape
    cols = shape[-1]
    lead = shape[0] if len(shape) >= 3 else 1
    rows = int(np.prod(shape[:-1])) // lead if len(shape) > 1 else 1
    w2, g2, m2, v2 = (a.reshape(lead, rows, cols) for a in (w, g, m, v))
    tr = rows
    for cand in (512, 256, 128, 64, 32, 16, 8):
        if rows % cand == 0 and rows > cand:
            tr = cand
            break
    bc1, bc2 = 1.0 - ADAM_B1 ** ADAM_STEP, 1.0 - ADAM_B2 ** ADAM_STEP

    def body(w_ref, g_ref, m_ref, v_ref, d_ref, nm_ref, nv_ref):
        gv = g_ref[...]
        nm = ADAM_B1 * m_ref[...] + (1.0 - ADAM_B1) * gv
        nv = ADAM_B2 * v_ref[...] + (1.0 - ADAM_B2) * (gv * gv)
        nm_ref[...] = nm
        nv_ref[...] = nv
        d_ref[...] = -ADAM_LR * ((nm / bc1) / (jnp.sqrt(nv / bc2) + ADAM_EPS) + ADAM_WD * w_ref[...])

    blk = pl.BlockSpec((1, tr, cols), lambda l, i: (l, i, 0))
    res = _call(body, name=name, grid=(lead, rows // tr), in_specs=[blk] * 4, out_specs=[blk] * 3,
                out_shape=[jax.ShapeDtypeStruct((lead, rows, cols), F32)] * 3, sem=("parallel", "parallel"),
                args=(w2, g2, m2, v2), comm=comm)
    outs, couts = res if comm is not None else (res, None)
    outs = tuple(o.reshape(shape) for o in outs)
    return outs if comm is None else (outs, couts)


WEIGHT_NAMES = ['norm_mix_g', 'norm_xa_g', 'norm_ffn_g', 'norm_mem_g', 'norm_final_g', 'w_in_ab', 'conv_qkv_a',
                'a_log_a', 'dt_bias_a', 'onorm_g_a', 'ssm_lambda_re', 'ssm_lambda_im', 'ssm_b_re', 'ssm_b_im',
                'ssm_c_re', 'ssm_c_im', 'ssm_d', 'ssm_log_dt', 'w_glu_b', 'b_glu_b', 'w_out_ab', 'pool_w',
                'pool_scale', 'xa_wq', 'xa_wkv', 'xa_wo', 'ffn_w_up', 'ffn_conv', 'ffn_w_down']
BIG_SHARDED = {'w_in_ab': ((1, 1024, 2568), 2), 'w_glu_b': ((1, 512, 512), 1), 'w_out_ab': ((1, 1024, 1024), 1),
               'pool_w': ((1, 4, 256, 256), 2), 'xa_wq': ((2, 1024, 1024), 1), 'xa_wkv': ((2, 1024, 2048), 2),
               'xa_wo': ((2, 1024, 1024), 1), 'ffn_w_up': ((2, 1024, 5632), 2), 'ffn_w_down': ((2, 2816, 1024), 1)}
SMALL_SHARDED = {'conv_qkv_a': ((1, 4, 1536), 2), 'pool_scale': ((1, 1024), 1), 'ffn_conv': ((2, 3, 5632), 2)}
REPLICATED = {'norm_mix_g': (2, 1024), 'norm_xa_g': (2, 1024), 'norm_ffn_g': (2, 1024), 'norm_mem_g': (1024,),
              'norm_final_g': (1024,), 'a_log_a': (1, 4), 'dt_bias_a': (1, 4), 'onorm_g_a': (1, 128),
              'ssm_lambda_re': (1, 32, 64), 'ssm_lambda_im': (1, 32, 64), 'ssm_b_re': (1, 32, 64, 16),
              'ssm_b_im': (1, 32, 64, 16), 'ssm_c_re': (1, 32, 16, 64), 'ssm_c_im': (1, 32, 16, 64),
              'ssm_d': (1, 32, 16), 'ssm_log_dt': (1, 32), 'b_glu_b': (1, 512)}
PACK_ROW_ALIGN = 8


def _shard_shape(shape, axis):
    return tuple(s // N_DEV if i == axis else s for i, s in enumerate(shape))


def _round_up(n, m):
    return (n + m - 1) // m * m


def _pack(arrays):
    total = sum(int(np.prod(a.shape)) for a in arrays)
    padded = _round_up(total, PACK_COLS * PACK_ROW_ALIGN)
    parts = [a.astype(F32).reshape(-1) for a in arrays]
    if padded != total:
        parts.append(jnp.zeros((padded - total,), F32))
    return jnp.concatenate(parts).reshape(padded // PACK_COLS, PACK_COLS)


def _unpack(packed, shapes):
    flat, out, off = packed.reshape(-1), [], 0
    for shape in shapes:
        size = int(np.prod(shape))
        out.append(flat[off:off + size].reshape(shape))
        off += size
    return out


def _split_shards(full, axis):
    shape = full.shape
    s = shape[axis] // N_DEV
    a = full.reshape(shape[:axis] + (N_DEV, s) + shape[axis + 1:])
    return jnp.moveaxis(a, axis, 0).reshape(N_DEV, -1)


def _merge_shards(pieces, shape, axis):
    sh = _shard_shape(shape, axis)
    a = pieces.reshape((N_DEV,) + sh)
    a = jnp.moveaxis(a, 0, axis)
    return a.reshape(shape)


_SCAN_NB = SSM_CH // SCAN_CB


def _to_scan_layout(m, axis):
    shape = m.shape
    m = m.reshape(shape[:axis] + (2, _SCAN_NB, SCAN_CB) + shape[axis + 1:])
    return jnp.swapaxes(m, axis, axis + 1).reshape(shape)


def _from_scan_layout(m, axis):
    shape = m.shape
    m = m.reshape(shape[:axis] + (_SCAN_NB, 2, SCAN_CB) + shape[axis + 1:])
    return jnp.swapaxes(m, axis, axis + 1).reshape(shape)


def _s5_discretise(lam_re, lam_im, b_re, b_im, log_dt):
    dt = jnp.exp(log_dt)[:, None]
    mag = jnp.exp(lam_re * dt)
    ang = lam_im * dt
    lb_re, lb_im = mag * jnp.cos(ang), mag * jnp.sin(ang)
    den = lam_re * lam_re + lam_im * lam_im
    nr, ni = lb_re - 1.0, lb_im
    coef_re = (nr * lam_re + ni * lam_im) / den
    coef_im = (ni * lam_re - nr * lam_im) / den
    bb_re = coef_re[..., None] * b_re - coef_im[..., None] * b_im
    bb_im = coef_re[..., None] * b_im + coef_im[..., None] * b_re
    return lb_re, lb_im, bb_re, bb_im


_GROUPS_PER_BLOCK = N_GROUPS // _SCAN_NB
_U_BLOCK = _GROUPS_PER_BLOCK * SSM_GROUP


def _s5_matrices(lb_re, lb_im, bb_re, bb_im, c_re, c_im):
    eye = jnp.eye(_GROUPS_PER_BLOCK, dtype=F32)
    blocked = lambda m: m.reshape((_SCAN_NB, _GROUPS_PER_BLOCK) + m.shape[1:])
    bmat = lambda bb: jnp.einsum('jgph,gk->jghkp', blocked(bb), eye).reshape(_SCAN_NB, _U_BLOCK, SCAN_CB)
    cmat = lambda cc: jnp.einsum('jghp,gk->jkpgh', blocked(cc), eye).reshape(_SCAN_NB, SCAN_CB, _U_BLOCK)
    b_in = jnp.concatenate([bmat(bb_re), bmat(bb_im)], axis=2)
    c_out = jnp.concatenate([cmat(c_re), -cmat(c_im)], axis=1)
    a_row = _to_scan_layout(jnp.concatenate([lb_re.reshape(1, SSM_CH), lb_im.reshape(1, SSM_CH)], axis=1), 1)
    return b_in, c_out, a_row


def _s5_matrix_grads(db_in, dc_out, da_row):
    da_nat = _from_scan_layout(da_row, 1)
    eye = jnp.eye(_GROUPS_PER_BLOCK, dtype=F32)
    nb, gb = _SCAN_NB, _GROUPS_PER_BLOCK
    bgrad = lambda m: jnp.einsum('jghkp,gk->jgph', m.reshape(nb, gb, SSM_GROUP, gb, SSM_STATE), eye
                                 ).reshape(N_GROUPS, SSM_STATE, SSM_GROUP)
    cgrad = lambda m: jnp.einsum('jkpgh,gk->jghp', m.reshape(nb, gb, SSM_STATE, gb, SSM_GROUP), eye
                                 ).reshape(N_GROUPS, SSM_GROUP, SSM_STATE)
    dbb_re, dbb_im = bgrad(db_in[:, :, :SCAN_CB]), bgrad(db_in[:, :, SCAN_CB:])
    dc_re, dc_im = cgrad(dc_out[:, :SCAN_CB]), -cgrad(dc_out[:, SCAN_CB:])
    dlb_re = da_nat[0, :SSM_CH].reshape(N_GROUPS, SSM_STATE)
    dlb_im = da_nat[0, SSM_CH:].reshape(N_GROUPS, SSM_STATE)
    return dlb_re, dlb_im, dbb_re, dbb_im, dc_re, dc_im


def _as_pieces(a):
    return a.reshape(N_DEV, a.shape[0] // N_DEV, a.shape[1])


def _hybrid_fwd(xn, x, wts, p, weights, riders):
    sv = {}
    hq = _mm(xn, wts['w_qkv_t'], "nt", "l0_in_qkv")
    gate = _mm(xn, wts['w_gate_t'], "nt", "l0_in_gate")
    ba = _mm(xn, wts['w_ba_t'], "nt", "l0_in_ba")
    u = _mm(xn, wts['w_u_t'], "nt", "l0_in_u")
    conv = p['conv_qkv']
    qkv = _qkv_pre_fwd(hq, conv, "l0_qkv_pre")
    gates = _gates_fwd(ba, p['arow'], p['brow'], "l0_gates")
    o, tm_all, s_all = riders.run("l0_gdr_fwd", _gdr_fwd, qkv, gates)
    wts['w_glu'], wts['w_out'] = weights.full['w_glu'], weights.full['w_out']
    y_a = _onorm_fwd(o, gate, p['onorm_g'], "l0_onorm")
    bu = riders.run("l0_s5_bu", _mm_bd, u, p['b_in'], "nn")
    xs = riders.run("l0_s5_scan", _s5_scan_fwd, bu, p['a_row'])
    weights.gather_by_sequencer(GATHER_LAYER1, xs, "gather_layer1", GATHER_LAYER1_ID)
    yc = riders.run("l0_s5_cx", _mm_bd, xs, p['c_out'], "nn")
    yl, y_b = _glu_fwd(yc, u, p['d_row'], wts['w_glu'], p['b_glu'], "l0_glu")
    mixed = jnp.concatenate([y_a, y_b], axis=1)
    x1 = _mm(mixed, wts['w_out'], "nn", "l0_out", res=x)
    sv.update(hq=hq, gate=gate, ba=ba, u=u, qkv=qkv, gb=gates, o=o, tm=tm_all, s=s_all, xs=xs, yl=yl, mixed=mixed)
    return x1, sv


def _hybrid_bwd(dx1, xn, wts, p, sv, riders):
    gr = {}
    dmixed = _mm(dx1, wts['w_out'], "nt", "l0_out_dx", out_dtype=BF16)
    riders.grad('w_out', _as_pieces(_mm(sv['mixed'], dx1, "tn", "l0_out_dw", out_dtype=BF16)))
    dya, dyb = dmixed[:, :WIDTH_A], dmixed[:, WIDTH_A:]
    dyl, du_direct, dw_glu, gr['b_glu_b'], dd = _glu_bwd(
        sv['yl'], sv['u'], p['d_row'], wts['w_glu'], p['b_glu'], dyb, "l0_glu_bwd")
    riders.grad('w_glu', dw_glu.astype(BF16).reshape(N_DEV, -1, PACK_COLS))
    dxs = riders.run("l0_s5_cx_dx", _mm_bd, dyl, p['c_out'], "nt")
    dc_out = _mm_bd(sv['xs'], dyl, "tn", "l0_s5_cx_dw")
    lam, da_row = riders.run("l0_s5_scan_bwd", _s5_scan_bwd, dxs, sv['xs'], p['a_row'])
    du = _mm_bd(lam, p['b_in'], "nt", "l0_s5_bu_dx", res=du_direct, out_dtype=BF16)
    db_in = _mm_bd(sv['u'], lam, "tn", "l0_s5_bu_dw")
    gr['s5'] = (db_in, dc_out, da_row, dd)
    do, dgate, gr['onorm_g_a'] = _onorm_bwd(sv['o'], sv['gate'], p['onorm_g'], dya, "l0_onorm_bwd")
    dqkv, dgb = riders.run("l0_gdr_bwd", _gdr_bwd, sv['qkv'], sv['gb'], sv['tm'], sv['s'], do)
    dhq, gr['conv_qkv_a'] = _qkv_pre_bwd(sv['hq'], p['conv_qkv'], dqkv, "l0_qkv_pre_bwd")
    dba, da_log, ddt_bias = _gates_bwd(sv['ba'], p['arow'], p['brow'], dgb, "l0_gates_bwd")
    gr['a_log_a'], gr['dt_bias_a'] = da_log[:, 4:8], ddt_bias[:, 4:8]
    dw_qkv_t = _mm(dhq, xn, "tn", "l0_in_qkv_dw", out_dtype=BF16)
    dw_gate_t = _mm(dgate, xn, "tn", "l0_in_gate_dw", out_dtype=BF16)
    dw_ba_t = _mm(dba, xn, "tn", "l0_in_ba_dw", out_dtype=BF16)
    dw_u_t = _mm(du, xn, "tn", "l0_in_u_dw", out_dtype=BF16)
    dw_in_t = _as_pieces(jnp.concatenate([dw_qkv_t, dw_gate_t, dw_ba_t[:8], dw_u_t], axis=0))
    riders.grad('w_in_t', jnp.concatenate(
        [dw_in_t, jnp.zeros((N_DEV, dict(PIECES)['w_in_t'] - W_IN_PIECE, D_MODEL), BF16)], axis=1))
    dxn = riders.run("l0_in_qkv_dx", _mm, dhq, wts['w_qkv_t'], "nn")
    dxn = _mm(dgate, wts['w_gate_t'], "nn", "l0_in_gate_dx", res=dxn)
    dxn = _mm(dba, wts['w_ba_t'], "nn", "l0_in_ba_dx", res=dxn)
    dxn = riders.run("l0_in_u_dx", _mm, du, wts['w_u_t'], "nn", res=dxn)
    return dxn, gr


def _xa_fwd(x1, g, mem_n, wq, wkv_t, wo, tag, riders):
    xq = _rms_fwd(x1, g, BF16, tag + "_norm")
    q = _mm(xq, wq, "nn", tag + "_q", out_dtype=BF16)
    kv = _mm(mem_n, wkv_t, "nt", tag + "_kv", out_dtype=BF16)
    o = riders.run(tag + "_attn", _attn_fwd, q, kv)
    x2 = _mm(o, wo, "nn", tag + "_o", res=x1)
    return x2, dict(xq=xq, q=q, kv=kv, o=o)


def _xa_bwd(dx2, x1, g, mem_n, wq, wkv_t, wo, sv, tag, layer, riders):
    do = _mm(dx2, wo, "nt", tag + "_o_dx", out_dtype=BF16)
    riders.grad('wo%d' % layer, _as_pieces(_mm(sv['o'], dx2, "tn", tag + "_o_dw", out_dtype=BF16)))
    dq, dk, dv = _attn_bwd(sv['q'], sv['kv'], do, tag + "_attn_bwd")
    dkv = jnp.concatenate([dk, dv], axis=1).astype(BF16)
    dxq = _mm(dq, wq, "nt", tag + "_q_dx")
    riders.grad('wq%d' % layer, _as_pieces(_mm(sv['xq'], dq, "tn", tag + "_q_dw", out_dtype=BF16)))
    dmem_n = _mm(dkv, wkv_t, "nn", tag + "_kv_dx")
    riders.grad('wkv_t%d' % layer, _as_pieces(_mm(dkv, mem_n, "tn", tag + "_kv_dw", out_dtype=BF16)))
    dx1, dg = riders.run(tag + "_norm_bwd", _rms_bwd, x1, g, dxq, dx2)
    return dx1, dmem_n, dg


def _ffn_fwd(x2, g, w_up_t, conv, w_down, tag, riders):
    xf = _rms_fwd(x2, g, BF16, tag + "_norm")
    h = riders.run(tag + "_up", _mm, xf, w_up_t, "nt")
    a = riders.run(tag + "_act", _ffn_act_fwd, h, conv)
    x3 = _mm(a, w_down, "nn", tag + "_down", res=x2)
    return x3, dict(xf=xf, h=h, a=a)


def _ffn_bwd(dx3, x2, g, w_up_t, conv, w_down, sv, tag, layer, riders):
    da = _mm(dx3, w_down, "nt", tag + "_down_dx")
    riders.grad('down%d' % layer, _as_pieces(_mm(sv['a'], dx3, "tn", tag + "_down_dw", out_dtype=BF16)))
    dh, dconv = riders.run(tag + "_act_bwd", _ffn_act_bwd, sv['h'], conv, da)
    dxf = riders.run(tag + "_up_dx", _mm, dh, w_up_t, "nn")
    dw_up_t = riders.run(tag + "_up_dw", _mm, dh, sv['xf'], "tn", out_dtype=BF16)
    riders.grad('up_t%d' % layer, _as_pieces(dw_up_t))
    dx2, dg = riders.run(tag + "_norm_bwd", _rms_bwd, x2, g, dxf, dx3)
    return dx2, dconv, dg


BIG_NAMES, SMALL_NAMES, REP_NAMES = list(BIG_SHARDED), list(SMALL_SHARDED), list(REPLICATED)
SMALL_SIZES = [int(np.prod(_shard_shape(*SMALL_SHARDED[n]))) for n in SMALL_NAMES]


PIECES = [('w_in_t', 384), ('w_glu', 32), ('w_out', 128), ('pool_w', 32), ('wq0', 128), ('wq1', 128),
          ('wkv_t0', 256), ('wkv_t1', 256), ('wo0', 128), ('wo1', 128), ('up_t0', 704), ('up_t1', 704),
          ('down0', 352), ('down1', 352)]
W_IN_ROWS = 4 * WIDTH_A + 2 * N_HEADS_A + SSM_WIDTH
W_IN_PIECE = W_IN_ROWS // N_DEV


def _row_tile(rows):
    return max(t for t in range(16, min(rows, 512) + 1, 16) if rows % t == 0)


class _Riders:
    def __init__(self):
        self.waiting = {}
        self.deferred = {}
        self.grads = {}
        self.groups = []
        self.reduced = {}

    def add(self, host, comm, then):
        self.waiting.setdefault(host, []).append((comm, then))

    def after(self, marker, then):
        self.deferred.setdefault(marker, []).append(then)

    def mark(self, name, out=None):
        for cont in self.deferred.pop(name, []):
            step = cont()
            if step is not None:
                values, then = step
                out, values = lax.optimization_barrier((out, values))
                then(values)
        return out

    def run(self, name, fn, *args, **kw):
        riders = self.waiting.pop(name, [])
        if not riders:
            out = fn(*args, name=name, **kw)
        else:
            out, couts = fn(*args, name=name, comm=[c for c, _ in riders], **kw)
            for (_, then), got in zip(riders, couts):
                then(got)
        return self.mark(name, out)

    def grad(self, key, pieces):
        self.grads[key] = pieces
        for group in [g for g in self.groups if all(k in self.grads for k in g[1])]:
            self.groups.remove(group)
            self._reduce(*group)

    def _reduce(self, name, keys, pair_marker, sum_marker):
        arrays = [self.grads[k] for k in keys]
        rows = sum(a.shape[1] for a in arrays)
        tile = _row_tile(rows)
        from_sibling = _sequencer_swap(arrays, name + "_to_sibling")

        def after_swap():
            if len(arrays) == 1:
                chip_sums = _pair_sum_pieces(arrays[0], from_sibling, name + "_pair_sum", tr=tile)
            else:
                core = lax.axis_index("c")
                keep = jnp.concatenate(
                    [lax.dynamic_index_in_dim(a.reshape(4, 2, a.shape[1], PACK_COLS), core, 1, keepdims=False)
                     for a in arrays], axis=1)
                chip_sums = _pair_sum(keep, from_sibling, name + "_pair_sum", tr=tile)

            def exchange_among_chips(chip_sums):
                from_chips = _sequencer_chips(chip_sums, name + "_to_chips")

                def store(total):
                    off = 0
                    for k, a in zip(keys, arrays):
                        self.reduced[k] = total[off:off + a.shape[1]]
                        off += a.shape[1]

                self.after(sum_marker, lambda: (
                    _chip_sum(chip_sums, from_chips, name + "_chip_sum", tr=tile), store))

            return chip_sums, exchange_among_chips

        self.after(pair_marker, after_swap)


class _Weights:
    def __init__(self, inp):
        bf = lambda a: a.astype(BF16)
        local = {'w_in_t': bf(inp['w_in_ab'][0]).T, 'w_glu': bf(inp['w_glu_b'][0]), 'w_out': bf(inp['w_out_ab'][0]),
                 'pool_w': bf(inp['pool_w'][0]),
                 'small': _pack([inp[n] for n in SMALL_NAMES])}
        for l in range(2):
            local['wq%d' % l] = bf(inp['xa_wq'][l])
            local['wkv_t%d' % l] = bf(inp['xa_wkv'][l]).T
            local['wo%d' % l] = bf(inp['xa_wo'][l])
            local['up_t%d' % l] = bf(inp['ffn_w_up'][l]).T
            local['down%d' % l] = bf(inp['ffn_w_down'][l])
        self.local, self.full = local, {}

    def plan(self, keys):
        return _gather_comm([self.local[k] for k in keys])

    def gather_by_sequencer(self, keys, after, name, collective_id):
        arrays = [self.local[k] for k in keys]
        tie = (after.reshape(-1)[0] * 0.0).astype(arrays[0].dtype)
        arrays[0] = arrays[0] + tie
        self.land(keys, _sequencer_gather(arrays, name, collective_id))

    def land(self, keys, gathered):
        for k, g in zip(keys, gathered):
            if k == 'small':
                off = 0
                for n, size in zip(SMALL_NAMES, SMALL_SIZES):
                    self.full[n] = _merge_shards(g.reshape(N_DEV, -1)[:, off:off + size], *SMALL_SHARDED[n])
                    off += size
            elif k == 'pool_w':
                self.full[k] = jnp.swapaxes(g, 0, 1).reshape(len(POOL_WINDOWS), POOL_GROUP, POOL_GROUP)
            else:
                self.full[k] = g.reshape(N_DEV * g.shape[1], g.shape[2])


GATHER_FIRST = ['w_in_t', 'small']
GATHER_LAYER0 = ['w_glu', 'w_out', 'wq0', 'wkv_t0', 'wo0', 'down0', 'up_t0']
GATHER_LAYER1 = ['pool_w', 'wq1', 'wkv_t1', 'wo1', 'up_t1', 'down1']
GATHER_LAYER0_ID, GATHER_LAYER1_ID = 7, 8
GRAD_RIDES = [('g_down1', ['down1'], 'l1_ffn_up_dx', 'l1_xa_norm_bwd'),
              ('g_up1', ['up_t1'], 'l1_xa_norm_bwd', 'l0_ffn_up_dx'),
              ('g_xa1', ['wq1', 'wkv_t1', 'wo1', 'pool_w'], 'l0_ffn_up_dx', 'l0_xa_norm_bwd'),
              ('g_down0', ['down0'], 'l0_ffn_up_dx', 'l0_s5_scan_bwd'),
              ('g_l0', ['up_t0', 'wq0', 'wkv_t0', 'wo0'], 'l0_s5_cx_dx', 'l0_in_u_dx'),
              ('g_out', ['w_out', 'w_glu'], 'l0_s5_scan_bwd', 'l0_in_u_dx'),
              ('g_in', ['w_in_t'], 'l0_in_u_dx', 'adamw_pool_w')]


def _local_step(inp):
    f32_of = lambda n: inp[n].astype(F32)
    weights = _Weights(inp)
    riders = _Riders()
    riders.groups = list(GRAD_RIDES)
    full = weights.full
    weights.land(GATHER_FIRST, _comm_only(weights.plan(GATHER_FIRST), "gather_first"))
    weights.gather_by_sequencer(GATHER_LAYER0, full['w_in_t'], "gather_layer0", GATHER_LAYER0_ID)
    w_in_t = full['w_in_t']
    wts0 = dict(w_qkv_t=w_in_t[:3 * WIDTH_A], w_gate_t=w_in_t[3 * WIDTH_A:4 * WIDTH_A],
                w_ba_t=jnp.concatenate([w_in_t[4 * WIDTH_A:4 * WIDTH_A + 8], jnp.zeros((LANE - 8, D_MODEL), BF16)], 0),
                w_u_t=w_in_t[4 * WIDTH_A + 8:])
    lb_disc, disc_vjp = jax.vjp(_s5_discretise, f32_of('ssm_lambda_re')[0], f32_of('ssm_lambda_im')[0],
                                f32_of('ssm_b_re')[0], f32_of('ssm_b_im')[0], f32_of('ssm_log_dt')[0])
    b_in, c_out, a_row = _s5_matrices(*lb_disc, f32_of('ssm_c_re')[0], f32_of('ssm_c_im')[0])
    zeros4 = jnp.zeros((1, 4), F32)
    p0 = dict(conv_qkv=full['conv_qkv_a'][0], onorm_g=f32_of('onorm_g_a'),
              arow=jnp.concatenate([zeros4, f32_of('a_log_a'), jnp.zeros((1, LANE - 8), F32)], 1),
              brow=jnp.concatenate([zeros4, f32_of('dt_bias_a'), jnp.zeros((1, LANE - 8), F32)], 1),
              b_in=b_in.astype(BF16), c_out=c_out.astype(BF16), a_row=a_row,
              d_row=f32_of('ssm_d').reshape(1, SSM_WIDTH), b_glu=f32_of('b_glu_b'))

    x0 = inp['x'][0]
    mem_n = _rms_fwd(inp['mem'][0], inp['norm_mem_g'], BF16, "mem_norm")
    xn0 = _rms_fwd(x0, inp['norm_mix_g'][0], BF16, "l0_mix_norm")
    x1, sv_mix0 = _hybrid_fwd(xn0, x0, wts0, p0, weights, riders)
    x2, sv_xa0 = _xa_fwd(x1, inp['norm_xa_g'][0], mem_n, full['wq0'], full['wkv_t0'], full['wo0'], "l0_xa", riders)
    x3, sv_ffn0 = _ffn_fwd(x2, inp['norm_ffn_g'][0], full['up_t0'], full['ffn_conv'][0], full['down0'], "l0_ffn", riders)
    xn1 = _rms_fwd(x3, inp['norm_mix_g'][1], F32, "l1_mix_norm")
    x4 = _pool_fwd(xn1, full['pool_w'], full['pool_scale'], x3, "l1_pool")
    x5, sv_xa1 = _xa_fwd(x4, inp['norm_xa_g'][1], mem_n, full['wq1'], full['wkv_t1'], full['wo1'], "l1_xa", riders)
    x6, sv_ffn1 = _ffn_fwd(x5, inp['norm_ffn_g'][1], full['up_t1'], full['ffn_conv'][1], full['down1'], "l1_ffn", riders)
    loss_part, dx6, dg_final = _loss_head(x6, inp['norm_final_g'], inp['loss_target'][0], "loss_head")

    dx5, dconv1, dg_ffn1 = _ffn_bwd(dx6, x5, inp['norm_ffn_g'][1], full['up_t1'], full['ffn_conv'][1], full['down1'],
                                    sv_ffn1, "l1_ffn", 1, riders)
    dx4, dmem1, dg_xa1 = _xa_bwd(dx5, x4, inp['norm_xa_g'][1], mem_n, full['wq1'], full['wkv_t1'], full['wo1'],
                                 sv_xa1, "l1_xa", 1, riders)
    dxn1, dpool_w, dpool_scale = _pool_bwd(xn1, full['pool_w'], full['pool_scale'], dx4, "l1_pool_bwd")
    pool_pieces = jnp.swapaxes(dpool_w.astype(BF16).reshape(len(POOL_WINDOWS), N_DEV, -1, POOL_GROUP), 0, 1)
    riders.grad('pool_w', pool_pieces.reshape(N_DEV, -1, PACK_COLS))
    dx3, dg_mix1 = riders.run("l1_mix_norm_bwd", _rms_bwd, x3, inp['norm_mix_g'][1], dxn1, dx4)
    dx2, dconv0, dg_ffn0 = _ffn_bwd(dx3, x2, inp['norm_ffn_g'][0], full['up_t0'], full['ffn_conv'][0], full['down0'],
                                    sv_ffn0, "l0_ffn", 0, riders)
    dx1, dmem0, dg_xa0 = _xa_bwd(dx2, x1, inp['norm_xa_g'][0], mem_n, full['wq0'], full['wkv_t0'], full['wo0'],
                                 sv_xa0, "l0_xa", 0, riders)
    dxn0, g_mix0 = _hybrid_bwd(dx1, xn0, wts0, p0, sv_mix0, riders)
    grad_x, dg_mix0 = _rms_bwd(x0, inp['norm_mix_g'][0], dxn0, dx1, "l0_mix_norm_bwd")
    _, dg_mem = _rms_bwd(inp['mem'][0], inp['norm_mem_g'], dmem0 + dmem1, None, "mem_norm_bwd")
    assert not riders.groups and not riders.waiting and all(k.startswith("adamw_") for k in riders.deferred), (
        riders.groups, list(riders.waiting), list(riders.deferred))

    db_in, dc_out, da_row, dd = g_mix0['s5']
    dlb_re, dlb_im, dbb_re, dbb_im, dc_re, dc_im = _s5_matrix_grads(db_in, dc_out, da_row)
    dlam_re, dlam_im, dbr, dbi, dlog_dt = disc_vjp((dlb_re, dlb_im, dbb_re, dbb_im))

    rep_grads = {
        'norm_mix_g': jnp.concatenate([dg_mix0, dg_mix1], 0), 'norm_xa_g': jnp.concatenate([dg_xa0, dg_xa1], 0),
        'norm_ffn_g': jnp.concatenate([dg_ffn0, dg_ffn1], 0), 'norm_mem_g': dg_mem.reshape(-1),
        'norm_final_g': dg_final.reshape(-1), 'a_log_a': g_mix0['a_log_a'], 'dt_bias_a': g_mix0['dt_bias_a'],
        'onorm_g_a': g_mix0['onorm_g_a'], 'ssm_lambda_re': dlam_re[None], 'ssm_lambda_im': dlam_im[None],
        'ssm_b_re': dbr[None], 'ssm_b_im': dbi[None], 'ssm_c_re': dc_re[None], 'ssm_c_im': dc_im[None],
        'ssm_d': dd.reshape(1, N_GROUPS, SSM_GROUP), 'ssm_log_dt': dlog_dt[None], 'b_glu_b': g_mix0['b_glu_b']}
    small_grads = {'conv_qkv_a': g_mix0['conv_qkv_a'][None], 'pool_scale': dpool_scale,
                   'ffn_conv': jnp.stack([dconv0, dconv1])}
    return loss_part, grad_x, riders, rep_grads, small_grads


ADAMW_ORDER = ['ffn_w_up', 'ffn_w_down', 'xa_wkv', 'xa_wq', 'xa_wo', 'w_out_ab', 'w_glu_b', 'pool_w', 'w_in_ab']


def _update(inp, loss_part, grad_x, riders, rep_grads, small_grads):
    dev = _device_index()
    misc_local = _pack([rep_grads[n] for n in REP_NAMES] + [small_grads[n] for n in SMALL_NAMES] + [loss_part])
    (misc_all,) = _sequencer_gather([misc_local], "gather_small_grads", GATHER_LAYER0_ID)
    piece = lambda key: riders.reduced[key]
    both = lambda name: jnp.stack([piece(name + '0'), piece(name + '1')])
    swap = lambda a: jnp.swapaxes(a, -1, -2)
    reduced = {'w_in_ab': lambda: piece('w_in_t')[:W_IN_PIECE][None],
               'w_glu_b': lambda: piece('w_glu').reshape(inp['w_glu_b'].shape),
               'w_out_ab': lambda: piece('w_out')[None], 'pool_w': lambda: piece('pool_w').reshape(inp['pool_w'].shape),
               'xa_wq': lambda: both('wq'), 'xa_wkv': lambda: both('wkv_t'), 'xa_wo': lambda: both('wo'),
               'ffn_w_up': lambda: both('up_t'), 'ffn_w_down': lambda: both('down')}
    transposed = ('w_in_ab', 'xa_wkv', 'ffn_w_up')
    grads, upd = {}, {}
    assert sorted(ADAMW_ORDER) == sorted(BIG_NAMES)
    for n in ADAMW_ORDER:
        fix = swap if n in transposed else (lambda a: a)
        g = reduced[n]()
        out = riders.run("adamw_" + n, _adamw, fix(inp[n]), g, fix(inp['m_' + n]), fix(inp['v_' + n]))
        upd[n], grads[n] = tuple(fix(o) for o in out), fix(g)
    assert not riders.waiting and not riders.deferred, (list(riders.waiting), list(riders.deferred))
    misc_sum = _sum_leading(misc_all, "small_grads_sum")
    misc = _unpack(misc_sum, [inp[n].shape for n in REP_NAMES] + [SMALL_SHARDED[n][0] for n in SMALL_NAMES] + [()])
    loss = misc.pop()
    for n, g in zip(REP_NAMES, misc):
        grads[n] = g
    for n, g in zip(SMALL_NAMES, misc[len(REP_NAMES):]):
        grads[n] = lax.dynamic_index_in_dim(_split_shards(g, SMALL_SHARDED[n][1]), dev, 0, keepdims=False
                                            ).reshape(inp[n].shape)
    tiny_names = REP_NAMES + SMALL_NAMES
    rep_total = sum(int(np.prod(inp[n].shape)) for n in REP_NAMES)
    packs = [_pack([inp[prefix + n] for n in tiny_names]) for prefix in ('', 'm_', 'v_')]
    g_pack = _pack([misc_sum.reshape(-1)[:rep_total]] + [grads[n] for n in SMALL_NAMES])
    tiny_out = [_unpack(o, [inp[n].shape for n in tiny_names])
                for o in _adamw(packs[0], g_pack, packs[1], packs[2], "adamw_small")]
    for i, n in enumerate(tiny_names):
        upd[n] = tuple(o[i] for o in tiny_out)

    outs = [loss, grad_x[None]]
    outs += [grads[n] for n in WEIGHT_NAMES]
    for i in range(3):
        outs += [upd[n][i] for n in WEIGHT_NAMES]
    return tuple(outs)


def _step(inp):
    loss_part, grad_x, riders, rep_grads, small_grads = _local_step(inp)
    return _update(inp, loss_part, grad_x, riders, rep_grads, small_grads)


INPUT_NAMES = (['x', 'mem'] + WEIGHT_NAMES + ['loss_target'] + ['m_' + n for n in WEIGHT_NAMES]
               + ['v_' + n for n in WEIGHT_NAMES])


def kernel(x, mem, norm_mix_g, norm_xa_g, norm_ffn_g, norm_mem_g, norm_final_g, w_in_ab, conv_qkv_a, a_log_a, dt_bias_a, onorm_g_a, ssm_lambda_re, ssm_lambda_im, ssm_b_re, ssm_b_im, ssm_c_re, ssm_c_im, ssm_d, ssm_log_dt, w_glu_b, b_glu_b, w_out_ab, pool_w, pool_scale, xa_wq, xa_wkv, xa_wo, ffn_w_up, ffn_conv, ffn_w_down, loss_target, m_norm_mix_g, m_norm_xa_g, m_norm_ffn_g, m_norm_mem_g, m_norm_final_g, m_w_in_ab, m_conv_qkv_a, m_a_log_a, m_dt_bias_a, m_onorm_g_a, m_ssm_lambda_re, m_ssm_lambda_im, m_ssm_b_re, m_ssm_b_im, m_ssm_c_re, m_ssm_c_im, m_ssm_d, m_ssm_log_dt, m_w_glu_b, m_b_glu_b, m_w_out_ab, m_pool_w, m_pool_scale, m_xa_wq, m_xa_wkv, m_xa_wo, m_ffn_w_up, m_ffn_conv, m_ffn_w_down, v_norm_mix_g, v_norm_xa_g, v_norm_ffn_g, v_norm_mem_g, v_norm_final_g, v_w_in_ab, v_conv_qkv_a, v_a_log_a, v_dt_bias_a, v_onorm_g_a, v_ssm_lambda_re, v_ssm_lambda_im, v_ssm_b_re, v_ssm_b_im, v_ssm_c_re, v_ssm_c_im, v_ssm_d, v_ssm_log_dt, v_w_glu_b, v_b_glu_b, v_w_out_ab, v_pool_w, v_pool_scale, v_xa_wq, v_xa_wkv, v_xa_wo, v_ffn_w_up, v_ffn_conv, v_ffn_w_down):
    args = (x, mem, norm_mix_g, norm_xa_g, norm_ffn_g, norm_mem_g, norm_final_g, w_in_ab, conv_qkv_a, a_log_a, dt_bias_a, onorm_g_a, ssm_lambda_re, ssm_lambda_im, ssm_b_re, ssm_b_im, ssm_c_re, ssm_c_im, ssm_d, ssm_log_dt, w_glu_b, b_glu_b, w_out_ab, pool_w, pool_scale, xa_wq, xa_wkv, xa_wo, ffn_w_up, ffn_conv, ffn_w_down, loss_target, m_norm_mix_g, m_norm_xa_g, m_norm_ffn_g, m_norm_mem_g, m_norm_final_g, m_w_in_ab, m_conv_qkv_a, m_a_log_a, m_dt_bias_a, m_onorm_g_a, m_ssm_lambda_re, m_ssm_lambda_im, m_ssm_b_re, m_ssm_b_im, m_ssm_c_re, m_ssm_c_im, m_ssm_d, m_ssm_log_dt, m_w_glu_b, m_b_glu_b, m_w_out_ab, m_pool_w, m_pool_scale, m_xa_wq, m_xa_wkv, m_xa_wo, m_ffn_w_up, m_ffn_conv, m_ffn_w_down, v_norm_mix_g, v_norm_xa_g, v_norm_ffn_g, v_norm_mem_g, v_norm_final_g, v_w_in_ab, v_conv_qkv_a, v_a_log_a, v_dt_bias_a, v_onorm_g_a, v_ssm_lambda_re, v_ssm_lambda_im, v_ssm_b_re, v_ssm_b_im, v_ssm_c_re, v_ssm_c_im, v_ssm_d, v_ssm_log_dt, v_w_glu_b, v_b_glu_b, v_w_out_ab, v_pool_w, v_pool_scale, v_xa_wq, v_xa_wkv, v_xa_wo, v_ffn_w_up, v_ffn_conv, v_ffn_w_down)
    return _step(dict(zip(INPUT_NAMES, args)))
```

```python
import functools
import math

import numpy as np
import jax
import jax.numpy as jnp
from jax import lax
from jax.experimental import pallas as pl
from jax.experimental.pallas import tpu as pltpu
from jax.experimental.pallas import tpu_sc as plsc

F32, BF16 = jnp.float32, jnp.bfloat16
HIGH, HIGHEST = lax.Precision.HIGH, lax.Precision.HIGHEST
MESH = pl.DeviceIdType.MESH

N_DEV = 8
SEQ, D_MODEL, MEM_LEN = 2048, 1024, 256
WIDTH_A, N_HEADS_A, HEAD_A, CONV_A = 512, 4, 128, 4
GDR_CHUNK = 128
SSM_WIDTH, SSM_GROUP, N_GROUPS, SSM_STATE = 512, 16, 32, 64
SSM_CH = N_GROUPS * SSM_STATE
SCAN_CB = 512
POOL_WINDOWS = (2, 4, 8, 16)
POOL_GROUP = 256
N_HEADS_X, HEAD_X = 4, 256
D_FF, CONV_FFN = 2816, 3
RMS_EPS = 1e-6
ADAM_LR, ADAM_B1, ADAM_B2, ADAM_EPS, ADAM_WD, ADAM_STEP = 0.001, 0.9, 0.999, 1e-08, 0.01, 10
LANE = 128
PACK_COLS = 1024
VMEM_LIMIT_BYTES = 56 * 1024 * 1024


def _params(sem=None):
    return pltpu.CompilerParams(dimension_semantics=sem, vmem_limit_bytes=VMEM_LIMIT_BYTES)


class Comm:
    def __init__(self, inputs, out_shapes, sems, start, end, mid=None):
        self.inputs, self.out_shapes, self.sems = list(inputs), list(out_shapes), list(sems)
        self.start, self.mid, self.end = start, mid, end


def _merge_comms(comms):
    comms = [c for c in comms if c is not None]
    if not comms:
        return None, []
    bounds, ni, no, ns = [], 0, 0, 0
    for c in comms:
        bounds.append((ni, no, ns))
        ni, no, ns = ni + len(c.inputs), no + len(c.out_shapes), ns + len(c.sems)

    def phase(which):
        def run(ins, outs, sems):
            for c, (i0, o0, s0) in zip(comms, bounds):
                fn = getattr(c, which)
                if fn is not None:
                    fn(ins[i0:i0 + len(c.inputs)], outs[o0:o0 + len(c.out_shapes)], sems[s0:s0 + len(c.sems)])
        return run

    merged = Comm([a for c in comms for a in c.inputs], [s for c in comms for s in c.out_shapes],
                  [s for c in comms for s in c.sems], phase("start"), phase("end"), phase("mid"))
    return merged, [(o0, o0 + len(c.out_shapes)) for c, (_, o0, _) in zip(comms, bounds)]


def _call(body, *, name, grid, in_specs, out_specs, out_shape, args, scratch_shapes=(), sem=None, comm=None):
    single = not isinstance(out_shape, (list, tuple))
    out_specs_l = [out_specs] if single else list(out_specs)
    out_shape_l = [out_shape] if single else list(out_shape)
    scratch_shapes = list(scratch_shapes)
    merged, spans = _merge_comms(comm if isinstance(comm, (list, tuple)) else [comm])
    if merged is None:
        outs = pl.pallas_call(body, name=name, grid=grid, in_specs=list(in_specs), out_specs=out_specs_l,
                              out_shape=out_shape_l, scratch_shapes=scratch_shapes, compiler_params=_params(sem))(*args)
        outs = outs[0] if single else outs
        return outs if comm is None else (outs, [])
    n_in, n_out, n_scr = len(in_specs), len(out_specs_l), len(scratch_shapes)
    ci, co = len(merged.inputs), len(merged.out_shapes)
    total = int(np.prod(grid))

    def wrapped(*refs):
        ins, cins = refs[:n_in], refs[n_in:n_in + ci]
        outs, couts = refs[n_in + ci:n_in + ci + n_out], refs[n_in + ci + n_out:n_in + ci + n_out + co]
        scr, csems = refs[n_in + ci + n_out + co:n_in + ci + n_out + co + n_scr], refs[n_in + ci + n_out + co + n_scr:]
        lin = pl.program_id(0)
        for d in range(1, len(grid)):
            lin = lin * grid[d] + pl.program_id(d)
        pl.when(lin == 0)(lambda: merged.start(cins, couts, csems))
        body(*ins, *outs, *scr)
        mid_step = min((3 * total) // 4, total - 1)
        pl.when(lin == mid_step)(lambda: merged.mid(cins, couts, csems))
        pl.when(lin == total - 1)(lambda: merged.end(cins, couts, csems))

    any_spec = pl.BlockSpec(memory_space=pl.ANY)
    res = pl.pallas_call(
        wrapped, name=name, grid=grid, in_specs=list(in_specs) + [any_spec] * ci,
        out_specs=out_specs_l + [any_spec] * co, out_shape=out_shape_l + merged.out_shapes,
        scratch_shapes=scratch_shapes + merged.sems,
        compiler_params=_params(("arbitrary",) * len(grid)))(*args, *merged.inputs)
    outs, couts = res[:n_out], res[n_out:]
    return (outs[0] if single else list(outs)), [list(couts[a:b]) for a, b in spans]


def _comm_only(comm, name):
    def body():
        pass

    _, couts = _call(body, name=name, grid=(1,), in_specs=[], out_specs=[], out_shape=[], args=[], comm=comm)
    return couts[0]


def _tile(dim, pref):
    best = None
    for t in range(LANE, min(dim, pref) + 1, LANE):
        if dim % t == 0:
            best = t
    return best if best is not None else dim


MM_VMEM_BUDGET = 40 * 1024 * 1024


def _mm_tiles(m, n, k, a_bytes, b_bytes, o_bytes, r_bytes):
    for tk in (k, _tile(k, 2048), _tile(k, 1024), _tile(k, 512)):
        for tm, tn in ((1024, 1536), (1024, 1024), (1024, 512), (512, 512), (256, 512), (256, 256)):
            tm, tn = _tile(m, tm), _tile(n, tn)
            acc = 0 if tk == k else tm * tn * 4
            need = 2 * (tm * tk * a_bytes + tk * tn * b_bytes + tm * tn * (o_bytes + r_bytes)) + acc
            if need <= MM_VMEM_BUDGET:
                return tm, tn, tk
    raise ValueError("no matmul tiling fits VMEM")


def _mm(a, b, mode, name, out_dtype=F32, res=None, comm=None):
    if mode == "nn":
        (m, k), n = a.shape, b.shape[1]
    elif mode == "nt":
        (m, k), n = a.shape, b.shape[0]
    else:
        (k, m), n = a.shape, b.shape[1]
    tm, tn, tk = _mm_tiles(m, n, k, a.dtype.itemsize, b.dtype.itemsize, jnp.dtype(out_dtype).itemsize,
                           0 if res is None else res.dtype.itemsize)
    nk = k // tk
    dims = {"nn": ((1,), (0,)), "nt": ((1,), (1,)), "tn": ((0,), (0,))}[mode]

    def body(*refs):
        if res is None:
            a_ref, b_ref, o_ref = refs[:3]
            r_ref = None
        else:
            a_ref, b_ref, r_ref, o_ref = refs[:4]
        part = lax.dot_general(a_ref[...].astype(BF16), b_ref[...].astype(BF16), (dims, ((), ())),
                               preferred_element_type=F32)

        def finish(out):
            if r_ref is not None:
                out = out + r_ref[...].astype(F32)
            o_ref[...] = out.astype(out_dtype)

        if nk == 1:
            finish(part)
            return
        acc = refs[-1]
        kk = pl.program_id(2)

        @pl.when(kk == 0)
        def _():
            acc[...] = part

        @pl.when(kk > 0)
        def _():
            acc[...] += part

        @pl.when(kk == nk - 1)
        def _():
            finish(acc[...])

    a_spec = (pl.BlockSpec((tk, tm), lambda i, j, q: (q, i)) if mode == "tn"
              else pl.BlockSpec((tm, tk), lambda i, j, q: (i, q)))
    b_spec = (pl.BlockSpec((tn, tk), lambda i, j, q: (j, q)) if mode == "nt"
              else pl.BlockSpec((tk, tn), lambda i, j, q: (q, j)))
    o_spec = pl.BlockSpec((tm, tn), lambda i, j, q: (i, j))
    in_specs, args = [a_spec, b_spec], [a, b]
    if res is not None:
        in_specs.append(o_spec)
        args.append(res)
    return _call(body, name=name, grid=(m // tm, n // tn, nk), in_specs=in_specs, out_specs=o_spec,
                 out_shape=jax.ShapeDtypeStruct((m, n), out_dtype),
                 scratch_shapes=[] if nk == 1 else [pltpu.VMEM((tm, tn), F32)],
                 sem=("parallel", "parallel", "arbitrary"), args=args, comm=comm)


def _mm_parts(a, b, mode, name, out_dtype=F32):
    parts, rows, cols = a.shape
    n = b.shape[1]
    if mode == "nn":
        tm, tn = _tile(rows, 1024), _tile(n, 512)

        def body(a_ref, b_ref, o_ref, acc):
            q = pl.program_id(2)
            part = jnp.dot(a_ref[0].astype(BF16), b_ref[...].astype(BF16), preferred_element_type=F32)

            @pl.when(q == 0)
            def _():
                acc[...] = part

            @pl.when(q > 0)
            def _():
                acc[...] += part

            @pl.when(q == parts - 1)
            def _():
                o_ref[...] = acc[...].astype(out_dtype)

        return _call(body, name=name, grid=(rows // tm, n // tn, parts),
                     in_specs=[pl.BlockSpec((1, tm, cols), lambda i, j, q: (q, i, 0)),
                               pl.BlockSpec((cols, tn), lambda i, j, q: (q, j))],
                     out_specs=pl.BlockSpec((tm, tn), lambda i, j, q: (i, j)),
                     out_shape=jax.ShapeDtypeStruct((rows, n), out_dtype), scratch_shapes=[pltpu.VMEM((tm, tn), F32)],
                     sem=("parallel", "parallel", "arbitrary"), args=(a, b))
    tm, tn = _tile(cols, 1536), _tile(n, 1024)
    per = cols // tm

    def body_t(a_ref, b_ref, o_ref):
        o_ref[...] = lax.dot_general(a_ref[0].astype(BF16), b_ref[...].astype(BF16), (((0,), (0,)), ((), ())),
                                     preferred_element_type=F32).astype(out_dtype)

    return _call(body_t, name=name, grid=(parts * per, n // tn),
                 in_specs=[pl.BlockSpec((1, rows, tm), lambda i, j: (i // per, 0, i % per)),
                           pl.BlockSpec((rows, tn), lambda i, j: (0, j))],
                 out_specs=pl.BlockSpec((tm, tn), lambda i, j: (i, j)),
                 out_shape=jax.ShapeDtypeStruct((parts * cols, n), out_dtype), sem=("parallel", "parallel"), args=(a, b))


def _mm_bd(a, b, mode, name, out_dtype=F32, res=None, comm=None, tm=1024):
    if mode == "tn":
        k = a.shape[0]
        nb = min(a.shape[1], b.shape[1]) // LANE
        ma, n = a.shape[1] // nb, b.shape[1] // nb

        def body(a_ref, b_ref, o_ref):
            o_ref[0] = lax.dot_general(a_ref[...].astype(BF16), b_ref[...].astype(BF16), (((0,), (0,)), ((), ())),
                                       preferred_element_type=F32).astype(out_dtype)

        return _call(body, name=name, grid=(nb,),
                     in_specs=[pl.BlockSpec((k, ma), lambda j: (0, j)), pl.BlockSpec((k, n), lambda j: (0, j))],
                     out_specs=pl.BlockSpec((1, ma, n), lambda j: (j, 0, 0)),
                     out_shape=jax.ShapeDtypeStruct((nb, ma, n), out_dtype), sem=("parallel",), args=(a, b), comm=comm)
    m = a.shape[0]
    nb = b.shape[0]
    ka = a.shape[1] // nb
    n = b.shape[2] if mode == "nn" else b.shape[1]
    tm = _tile(m, tm)
    dims = ((1,), (0,)) if mode == "nn" else ((1,), (1,))

    def body(*refs):
        if res is None:
            a_ref, b_ref, o_ref = refs
            r_ref = None
        else:
            a_ref, b_ref, r_ref, o_ref = refs
        out = lax.dot_general(a_ref[...].astype(BF16), b_ref[0].astype(BF16), (dims, ((), ())),
                              preferred_element_type=F32)
        if r_ref is not None:
            out = out + r_ref[...].astype(F32)
        o_ref[...] = out.astype(out_dtype)

    o_spec = pl.BlockSpec((tm, n), lambda i, j: (i, j))
    in_specs = [pl.BlockSpec((tm, ka), lambda i, j: (i, j)), pl.BlockSpec((1,) + b.shape[1:], lambda i, j: (j, 0, 0))]
    args = [a, b]
    if res is not None:
        in_specs.append(o_spec)
        args.append(res)
    return _call(body, name=name, grid=(m // tm, nb), in_specs=in_specs, out_specs=o_spec,
                 out_shape=jax.ShapeDtypeStruct((m, nb * n), out_dtype), sem=("parallel", "parallel"),
                 args=args, comm=comm)


def _rms_fwd(x, g, out_dtype, name, tr=256):
    rows, d = x.shape

    def body(x_ref, g_ref, o_ref):
        xv = x_ref[...]
        r = lax.rsqrt(jnp.mean(xv * xv, axis=-1, keepdims=True) + RMS_EPS)
        o_ref[...] = (xv * r * g_ref[...]).astype(out_dtype)

    return pl.pallas_call(
        body, name=name, grid=(rows // tr,),
        in_specs=[pl.BlockSpec((tr, d), lambda i: (i, 0)), pl.BlockSpec((1, d), lambda i: (0, 0))],
        out_specs=pl.BlockSpec((tr, d), lambda i: (i, 0)), out_shape=jax.ShapeDtypeStruct((rows, d), out_dtype),
        compiler_params=_params(("parallel",)))(x, g.reshape(1, d))


def _rms_bwd(x, g, dy, dres, name, tr=256, comm=None):
    rows, d = x.shape

    def body(*refs):
        if dres is None:
            x_ref, g_ref, dy_ref, dx_ref, dg_ref = refs
            r_ref = None
        else:
            x_ref, g_ref, dy_ref, r_ref, dx_ref, dg_ref = refs

        @pl.when(pl.program_id(0) == 0)
        def _():
            dg_ref[...] = jnp.zeros_like(dg_ref)

        xv, dyv = x_ref[...], dy_ref[...].astype(F32)
        r = lax.rsqrt(jnp.mean(xv * xv, axis=-1, keepdims=True) + RMS_EPS)
        xh = xv * r
        dyg = dyv * g_ref[...]
        dx = r * (dyg - xh * jnp.mean(dyg * xh, axis=-1, keepdims=True))
        if r_ref is not None:
            dx = dx + r_ref[...]
        dx_ref[...] = dx
        dg_ref[...] += jnp.sum(dyv * xh, axis=0, keepdims=True)

    blk = pl.BlockSpec((tr, d), lambda i: (i, 0))
    vec = pl.BlockSpec((1, d), lambda i: (0, 0))
    in_specs, args = [blk, vec, blk], [x, g.reshape(1, d), dy]
    if dres is not None:
        in_specs.append(blk)
        args.append(dres)
    return _call(
        body, name=name, grid=(rows // tr,), in_specs=in_specs, out_specs=[blk, vec],
        out_shape=[jax.ShapeDtypeStruct((rows, d), F32), jax.ShapeDtypeStruct((1, d), F32)],
        sem=("arbitrary",), args=args, comm=comm)


def _loss_head(x, g, target, name, tr=256):
    rows, d = x.shape

    def body(x_ref, g_ref, t_ref, loss_ref, dx_ref, dg_ref):
        @pl.when(pl.program_id(0) == 0)
        def _():
            dg_ref[...] = jnp.zeros_like(dg_ref)
            loss_ref[...] = jnp.zeros_like(loss_ref)

        xv = x_ref[...]
        r = lax.rsqrt(jnp.mean(xv * xv, axis=-1, keepdims=True) + RMS_EPS)
        xh = xv * r
        err = xh * g_ref[...] - t_ref[...]
        loss_ref[...] += 0.5 * jnp.sum(jnp.mean(err * err, axis=-1, keepdims=True), keepdims=True)
        dyv = err * (1.0 / d)
        dyg = dyv * g_ref[...]
        dx_ref[...] = r * (dyg - xh * jnp.mean(dyg * xh, axis=-1, keepdims=True))
        dg_ref[...] += jnp.sum(dyv * xh, axis=0, keepdims=True)

    blk = pl.BlockSpec((tr, d), lambda i: (i, 0))
    vec = pl.BlockSpec((1, d), lambda i: (0, 0))
    return pl.pallas_call(
        body, name=name, grid=(rows // tr,), in_specs=[blk, vec, blk],
        out_specs=[pl.BlockSpec((1, 1), lambda i: (0, 0)), blk, vec],
        out_shape=[jax.ShapeDtypeStruct((1, 1), F32), jax.ShapeDtypeStruct((rows, d), F32),
                   jax.ShapeDtypeStruct((1, d), F32)],
        compiler_params=_params(("arbitrary",)))(x, g.reshape(1, d), target)


def _shift_down(x, s):
    rows = lax.broadcasted_iota(jnp.int32, x.shape, 0)
    return jnp.where(rows >= s, pltpu.roll(x, s, 0), 0.0)


def _shift_up(x, s):
    n = x.shape[0]
    rows = lax.broadcasted_iota(jnp.int32, x.shape, 0)
    return jnp.where(rows < n - s, pltpu.roll(x, n - s, 0), 0.0)


def _sigmoid(x):
    return 1.0 / (1.0 + jnp.exp(-x))


def _silu_and_grad(x):
    s = _sigmoid(x)
    return x * s, s * (1.0 + x * (1.0 - s))


_GELU_C0, _GELU_C1 = math.sqrt(2.0 / math.pi), 0.044715


def _gelu_and_grad(x):
    th = jnp.tanh(_GELU_C0 * (x + _GELU_C1 * x * x * x))
    y = 0.5 * x * (1.0 + th)
    dy = 0.5 * (1.0 + th) + 0.5 * x * (1.0 - th * th) * _GELU_C0 * (1.0 + 3.0 * _GELU_C1 * x * x)
    return y, dy


def _ffn_act_fwd(h, w, name, tc=256, comm=None):
    t = h.shape[0]
    nb = D_FF // tc

    def body(hg_ref, hv_ref, wg_ref, wv_ref, a_ref):
        def conv(x, wr):
            return wr[2:3, :] * x + wr[1:2, :] * _shift_down(x, 1) + wr[0:1, :] * _shift_down(x, 2)

        cg = conv(hg_ref[...], wg_ref[...])
        cv = conv(hv_ref[...], wv_ref[...])
        a_ref[...] = (cg * _sigmoid(cg) * cv).astype(BF16)

    return _call(
        body, name=name, grid=(nb,),
        in_specs=[pl.BlockSpec((t, tc), lambda j: (0, j)), pl.BlockSpec((t, tc), lambda j: (0, j + nb)),
                  pl.BlockSpec((CONV_FFN, tc), lambda j: (0, j)), pl.BlockSpec((CONV_FFN, tc), lambda j: (0, j + nb))],
        out_specs=pl.BlockSpec((t, tc), lambda j: (0, j)), out_shape=jax.ShapeDtypeStruct((t, D_FF), BF16),
        sem=("parallel",), args=(h, h, w, w), comm=comm)


def _ffn_act_bwd(h, w, da, name, tc=256, comm=None):
    t = h.shape[0]
    nb = D_FF // tc

    def body(hg_ref, hv_ref, wg_ref, wv_ref, da_ref, dh_ref, dwg_ref, dwv_ref):
        hg, hv, wg, wv = hg_ref[...], hv_ref[...], wg_ref[...], wv_ref[...]
        hg1, hg2, hv1, hv2 = _shift_down(hg, 1), _shift_down(hg, 2), _shift_down(hv, 1), _shift_down(hv, 2)
        cg = wg[2:3, :] * hg + wg[1:2, :] * hg1 + wg[0:1, :] * hg2
        cv = wv[2:3, :] * hv + wv[1:2, :] * hv1 + wv[0:1, :] * hv2
        sg, dsg = _silu_and_grad(cg)
        dav = da_ref[...].astype(F32)
        dcv = dav * sg
        dcg = dav * cv * dsg

        def conv_t(dc, wr):
            return wr[2:3, :] * dc + wr[1:2, :] * _shift_up(dc, 1) + wr[0:1, :] * _shift_up(dc, 2)

        dh_ref[0] = conv_t(dcg, wg).astype(BF16)
        dh_ref[1] = conv_t(dcv, wv).astype(BF16)
        dwg_ref[0:1, :] = jnp.sum(dcg * hg2, axis=0, keepdims=True)
        dwg_ref[1:2, :] = jnp.sum(dcg * hg1, axis=0, keepdims=True)
        dwg_ref[2:3, :] = jnp.sum(dcg * hg, axis=0, keepdims=True)
        dwv_ref[0:1, :] = jnp.sum(dcv * hv2, axis=0, keepdims=True)
        dwv_ref[1:2, :] = jnp.sum(dcv * hv1, axis=0, keepdims=True)
        dwv_ref[2:3, :] = jnp.sum(dcv * hv, axis=0, keepdims=True)

    big = lambda off: pl.BlockSpec((t, tc), lambda j: (0, j + off))
    small = lambda off: pl.BlockSpec((CONV_FFN, tc), lambda j: (0, j + off))
    res = _call(
        body, name=name, grid=(nb,),
        in_specs=[big(0), big(nb), small(0), small(nb), big(0)],
        out_specs=[pl.BlockSpec((2, t, tc), lambda j: (0, 0, j)), small(0), small(0)],
        out_shape=[jax.ShapeDtypeStruct((2, t, D_FF), BF16),
                   jax.ShapeDtypeStruct((CONV_FFN, D_FF), F32), jax.ShapeDtypeStruct((CONV_FFN, D_FF), F32)],
        sem=("parallel",), args=(h, h, w, w, da), comm=comm)
    (dh, dwg, dwv), couts = res if comm is not None else (res, None)
    out = (dh, jnp.concatenate([dwg, dwv], axis=1))
    return out if comm is None else (out, couts)


def _attn_probs(q, k):
    s = lax.dot_general(q.astype(BF16), k.astype(BF16), (((1,), (1,)), ((), ())),
                        preferred_element_type=F32) * (HEAD_X ** -0.5)
    s = s - jnp.max(s, axis=-1, keepdims=True)
    p = jnp.exp(s)
    return p / jnp.sum(p, axis=-1, keepdims=True)


def _attn_fwd(q, kv, name, tq=512, comm=None):
    t = q.shape[0]

    def body(q_ref, k_ref, v_ref, o_ref):
        p = _attn_probs(q_ref[...], k_ref[...])
        o_ref[...] = jnp.dot(p.astype(BF16), v_ref[...].astype(BF16), preferred_element_type=F32).astype(BF16)

    return _call(
        body, name=name, grid=(N_HEADS_X, t // tq),
        in_specs=[pl.BlockSpec((tq, HEAD_X), lambda h, i: (i, h)),
                  pl.BlockSpec((MEM_LEN, HEAD_X), lambda h, i: (0, h)),
                  pl.BlockSpec((MEM_LEN, HEAD_X), lambda h, i: (0, h + N_HEADS_X))],
        out_specs=pl.BlockSpec((tq, HEAD_X), lambda h, i: (i, h)),
        out_shape=jax.ShapeDtypeStruct((t, N_HEADS_X * HEAD_X), BF16),
        sem=("parallel", "parallel"), args=(q, kv, kv), comm=comm)


def _attn_bwd(q, kv, do, name, tq=512):
    t = q.shape[0]

    def body(q_ref, k_ref, v_ref, do_ref, dq_ref, dk_ref, dv_ref):
        @pl.when(pl.program_id(1) == 0)
        def _():
            dk_ref[...] = jnp.zeros_like(dk_ref)
            dv_ref[...] = jnp.zeros_like(dv_ref)

        qb, kb, vb, dob = (r[...].astype(BF16) for r in (q_ref, k_ref, v_ref, do_ref))
        p = _attn_probs(qb, kb)
        dp = lax.dot_general(dob, vb, (((1,), (1,)), ((), ())), preferred_element_type=F32)
        ds = p * (dp - jnp.sum(dp * p, axis=-1, keepdims=True)) * (HEAD_X ** -0.5)
        dsb = ds.astype(BF16)
        dq_ref[...] = jnp.dot(dsb, kb, preferred_element_type=F32).astype(BF16)
        dk_ref[...] += lax.dot_general(dsb, qb, (((0,), (0,)), ((), ())), preferred_element_type=F32)
        dv_ref[...] += lax.dot_general(p.astype(BF16), dob, (((0,), (0,)), ((), ())), preferred_element_type=F32)

    qs = pl.BlockSpec((tq, HEAD_X), lambda h, i: (i, h))
    ms = pl.BlockSpec((MEM_LEN, HEAD_X), lambda h, i: (0, h))
    return pl.pallas_call(
        body, name=name, grid=(N_HEADS_X, t // tq),
        in_specs=[qs, ms, pl.BlockSpec((MEM_LEN, HEAD_X), lambda h, i: (0, h + N_HEADS_X)), qs],
        out_specs=[qs, ms, ms],
        out_shape=[jax.ShapeDtypeStruct((t, D_MODEL), BF16), jax.ShapeDtypeStruct((MEM_LEN, D_MODEL), F32),
                   jax.ShapeDtypeStruct((MEM_LEN, D_MODEL), F32)],
        compiler_params=_params(("parallel", "arbitrary")))(q, kv, kv, do)


def _pool_counts(t, win):
    pos = lax.broadcasted_iota(jnp.int32, (t, 1), 0).astype(F32) + 1.0
    return 1.0 / jnp.minimum(pos, float(win))


def _pool_delta(xv, win):
    s, step = xv, 1
    while step < win:
        s = s + _shift_down(s, step)
        step *= 2
    return s * _pool_counts(xv.shape[0], win) - xv


def _pool_delta_t(dv, win):
    s, step = dv * _pool_counts(dv.shape[0], win), 1
    while step < win:
        s = s + _shift_up(s, step)
        step *= 2
    return s - dv


def _pool_fwd(xn, w, scale, res, name):
    t = xn.shape[0]

    def make_branch(win, xn_ref, w_ref, s_ref, r_ref, o_ref):
        def branch():
            dl = _pool_delta(xn_ref[...], win)
            y = jnp.dot(dl.astype(BF16), w_ref[0], preferred_element_type=F32)
            o_ref[...] = r_ref[...] + y * s_ref[...]
        return branch

    def body(xn_ref, w_ref, s_ref, r_ref, o_ref):
        for gi, win in enumerate(POOL_WINDOWS):
            pl.when(pl.program_id(0) == gi)(make_branch(win, xn_ref, w_ref, s_ref, r_ref, o_ref))

    blk = pl.BlockSpec((t, POOL_GROUP), lambda g: (0, g))
    return pl.pallas_call(
        body, name=name, grid=(len(POOL_WINDOWS),),
        in_specs=[blk, pl.BlockSpec((1, POOL_GROUP, POOL_GROUP), lambda g: (g, 0, 0)),
                  pl.BlockSpec((1, POOL_GROUP), lambda g: (0, g)), blk],
        out_specs=blk, out_shape=jax.ShapeDtypeStruct((t, D_MODEL), F32),
        compiler_params=_params(("parallel",)))(xn, w, scale, res)


def _pool_bwd(xn, w, scale, dmix, name):
    t = xn.shape[0]

    def make_branch(win, xn_ref, w_ref, s_ref, d_ref, dxn_ref, dw_ref, ds_ref):
        def branch():
            dl = _pool_delta(xn_ref[...], win).astype(BF16)
            wv = w_ref[0]
            dm = d_ref[...]
            y = jnp.dot(dl, wv, preferred_element_type=F32)
            ds_ref[...] = jnp.sum(dm * y, axis=0, keepdims=True)
            dy = (dm * s_ref[...]).astype(BF16)
            dw_ref[0] = lax.dot_general(dl, dy, (((0,), (0,)), ((), ())), preferred_element_type=F32)
            ddl = lax.dot_general(dy, wv, (((1,), (1,)), ((), ())), preferred_element_type=F32)
            dxn_ref[...] = _pool_delta_t(ddl, win)
        return branch

    def body(*refs):
        for gi, win in enumerate(POOL_WINDOWS):
            pl.when(pl.program_id(0) == gi)(make_branch(win, *refs))

    blk = pl.BlockSpec((t, POOL_GROUP), lambda g: (0, g))
    wspec = pl.BlockSpec((1, POOL_GROUP, POOL_GROUP), lambda g: (g, 0, 0))
    vec = pl.BlockSpec((1, POOL_GROUP), lambda g: (0, g))
    return pl.pallas_call(
        body, name=name, grid=(len(POOL_WINDOWS),), in_specs=[blk, wspec, vec, blk], out_specs=[blk, wspec, vec],
        out_shape=[jax.ShapeDtypeStruct((t, D_MODEL), F32),
                   jax.ShapeDtypeStruct((len(POOL_WINDOWS), POOL_GROUP, POOL_GROUP), F32),
                   jax.ShapeDtypeStruct((1, D_MODEL), F32)],
        compiler_params=_params(("parallel",)))(xn, w, scale, dmix)


def _qkv_conv(h, wr):
    return (wr[3:4, :] * h + wr[2:3, :] * _shift_down(h, 1) + wr[1:2, :] * _shift_down(h, 2)
            + wr[0:1, :] * _shift_down(h, 3))


def _qkv_block_kind(j):
    return j < 2 * N_HEADS_A, jnp.where(j < N_HEADS_A, HEAD_A ** -0.5, 1.0)


def _qkv_pre_fwd(h, w, name):
    t, cols = h.shape

    def body(h_ref, w_ref, o_ref):
        normalised, scale = _qkv_block_kind(pl.program_id(0))
        c = _qkv_conv(h_ref[...], w_ref[...])
        s = c * _sigmoid(c)
        r = lax.rsqrt(jnp.sum(s * s, axis=-1, keepdims=True) + 1e-6)
        o_ref[...] = jnp.where(normalised, s * (r * scale), s)

    blk = pl.BlockSpec((t, HEAD_A), lambda j: (0, j))
    return pl.pallas_call(
        body, name=name, grid=(cols // HEAD_A,), in_specs=[blk, pl.BlockSpec((CONV_A, HEAD_A), lambda j: (0, j))],
        out_specs=blk, out_shape=jax.ShapeDtypeStruct((t, cols), F32), compiler_params=_params(("parallel",)))(h, w)


def _qkv_pre_bwd(h, w, dy, name):
    t, cols = h.shape

    def body(h_ref, w_ref, dy_ref, dh_ref, dw_ref):
        normalised, scale = _qkv_block_kind(pl.program_id(0))
        hv, wr, dyv = h_ref[...], w_ref[...], dy_ref[...]
        h1, h2, h3 = _shift_down(hv, 1), _shift_down(hv, 2), _shift_down(hv, 3)
        c = wr[3:4, :] * hv + wr[2:3, :] * h1 + wr[1:2, :] * h2 + wr[0:1, :] * h3
        s, dsilu = _silu_and_grad(c)
        r = lax.rsqrt(jnp.sum(s * s, axis=-1, keepdims=True) + 1e-6)
        y = s * r
        dys = dyv * scale
        ds = jnp.where(normalised, r * (dys - y * jnp.sum(dys * y, axis=-1, keepdims=True)), dyv)
        dc = ds * dsilu
        dh = (wr[3:4, :] * dc + wr[2:3, :] * _shift_up(dc, 1) + wr[1:2, :] * _shift_up(dc, 2)
              + wr[0:1, :] * _shift_up(dc, 3))
        dh_ref[...] = dh.astype(BF16)
        dw_ref[0:1, :] = jnp.sum(dc * h3, axis=0, keepdims=True)
        dw_ref[1:2, :] = jnp.sum(dc * h2, axis=0, keepdims=True)
        dw_ref[2:3, :] = jnp.sum(dc * h1, axis=0, keepdims=True)
        dw_ref[3:4, :] = jnp.sum(dc * hv, axis=0, keepdims=True)

    blk = pl.BlockSpec((t, HEAD_A), lambda j: (0, j))
    taps = pl.BlockSpec((CONV_A, HEAD_A), lambda j: (0, j))
    return pl.pallas_call(
        body, name=name, grid=(cols // HEAD_A,), in_specs=[blk, taps, blk], out_specs=[blk, taps],
        out_shape=[jax.ShapeDtypeStruct((t, cols), BF16), jax.ShapeDtypeStruct((CONV_A, cols), F32)],
        compiler_params=_params(("parallel",)))(h, w, dy)


def _softplus(x):
    return jnp.maximum(x, 0.0) + jnp.log1p(jnp.exp(-jnp.abs(x)))


def _gates_fwd(ba, arow, brow, name):
    t = ba.shape[0]

    def body(x_ref, a_ref, b_ref, o_ref):
        xv = x_ref[...]
        lane = lax.broadcasted_iota(jnp.int32, xv.shape, 1)
        beta = _sigmoid(xv)
        g = -jnp.exp(a_ref[...]) * _softplus(xv + b_ref[...])
        o_ref[...] = jnp.where(lane < N_HEADS_A, beta, jnp.where(lane < 2 * N_HEADS_A, g, 0.0))

    return pl.pallas_call(body, name=name, out_shape=jax.ShapeDtypeStruct((t, LANE), F32),
                          compiler_params=_params())(ba, arow, brow)


def _gates_bwd(ba, arow, brow, dgb, name):
    t = ba.shape[0]

    def body(x_ref, a_ref, b_ref, d_ref, dx_ref, da_ref, db_ref):
        xv = x_ref[...]
        dv = d_ref[0] + d_ref[1] + d_ref[2] + d_ref[3]
        lane = lax.broadcasted_iota(jnp.int32, xv.shape, 1)
        beta = _sigmoid(xv)
        ea = jnp.exp(a_ref[...])
        z = xv + b_ref[...]
        dgv = jnp.where((lane >= N_HEADS_A) & (lane < 2 * N_HEADS_A), dv, 0.0) * (-ea)
        dz = dgv * _sigmoid(z)
        dx = jnp.where(lane < N_HEADS_A, dv * beta * (1.0 - beta), dz)
        dx_ref[...] = dx.astype(BF16)
        db_ref[...] = jnp.sum(dz, axis=0, keepdims=True)
        da_ref[...] = jnp.sum(dgv * _softplus(z), axis=0, keepdims=True)

    return pl.pallas_call(
        body, name=name,
        out_shape=[jax.ShapeDtypeStruct((t, LANE), BF16), jax.ShapeDtypeStruct((1, LANE), F32),
                   jax.ShapeDtypeStruct((1, LANE), F32)],
        compiler_params=_params())(ba, arow, brow, dgb)


def _head_gates(gates, head):
    lane = lax.broadcasted_iota(jnp.int32, gates.shape, 1)
    beta = jnp.sum(jnp.where(lane == head, gates, 0.0), axis=1, keepdims=True)
    g = jnp.sum(jnp.where(lane == head + N_HEADS_A, gates, 0.0), axis=1, keepdims=True)
    return beta, g


_B_NN, _B_NT, _B_TN = ((2,), (1,)), ((2,), (2,)), ((1,), (1,))


def _bdot(a, b, dims=_B_NN, prec=None):
    if prec is None:
        a, b = a.astype(BF16), b.astype(BF16)
    return lax.dot_general(a, b, (dims, ((0,), (0,))), precision=prec, preferred_element_type=F32)


def _heads_of(ref):
    return jnp.stack([ref[:, h * HEAD_A:(h + 1) * HEAD_A] for h in range(N_HEADS_A)])


def _all_head_gates(gates):
    pairs = [_head_gates(gates, h) for h in range(N_HEADS_A)]
    return jnp.stack([b for b, _ in pairs]), jnp.stack([g for _, g in pairs])


def _gdr_terms(k, beta, g):
    h, c = k.shape[0], GDR_CHUNK
    row = lax.broadcasted_iota(jnp.int32, (c, c), 0)
    col = lax.broadcasted_iota(jnp.int32, (c, c), 1)
    causal, strict = row >= col, row > col
    lower = jnp.broadcast_to(causal.astype(F32), (h, c, c))
    gcum = _bdot(lower, jnp.broadcast_to(g, (h, c, c)), prec=HIGHEST)
    diff = gcum - jnp.swapaxes(gcum, 1, 2)
    decay = jnp.where(causal, jnp.exp(jnp.where(causal, diff, 0.0)), 0.0)
    kb = k * beta
    return row, col, causal, strict, gcum, decay, kb, _bdot(kb, k, _B_NT)


def _unit_lower_inverses(a):
    c = a.shape[1]
    eye = (lax.broadcasted_iota(jnp.int32, (c, c), 0) == lax.broadcasted_iota(jnp.int32, (c, c), 1)).astype(F32)
    p = -a
    inv = eye + p
    step = 1
    while 2 * step < c:
        p = _bdot(p, p, prec=HIGH)
        inv = inv + _bdot(inv, p, prec=HIGH)
        step *= 2
    return inv


def _gdr_fwd(qkv, gates, name, comm=None):
    t = qkv.shape[0]
    c, nh = GDR_CHUNK, N_HEADS_A
    n = t // c

    def body(q_ref, k_ref, v_ref, gb_ref, o_ref, tm_ref, s_ref, state):
        @pl.when(pl.program_id(0) == 0)
        def _():
            state[...] = jnp.zeros_like(state)

        qv, kv, vv = _heads_of(q_ref), _heads_of(k_ref), _heads_of(v_ref)
        beta, g = _all_head_gates(gb_ref[...])
        row, col, causal, strict, gcum, decay, kb, kk = _gdr_terms(kv, beta, g)
        tm = _unit_lower_inverses(jnp.where(strict, kk * decay, 0.0))
        e = jnp.exp(gcum)
        u = _bdot(tm, vv * beta, prec=HIGH)
        w = _bdot(tm, kb * e, prec=HIGH)
        p = jnp.where(causal, _bdot(qv, kv, _B_NT) * decay, 0.0)
        s = state[...]
        s_ref[:, 0] = s
        tm_ref[:, 0] = tm
        vn = u - _bdot(w, s)
        o = _bdot(qv * e, s) + _bdot(p, vn)
        for h in range(nh):
            o_ref[:, h * HEAD_A:(h + 1) * HEAD_A] = o[h]
        glast = gcum[:, c - 1:c, :]
        state[...] = s * jnp.exp(glast) + _bdot(kv * jnp.exp(glast - gcum), vn, _B_TN)

    part = lambda p: pl.BlockSpec((c, WIDTH_A), lambda i: (i, p))
    mat = pl.BlockSpec((nh, 1, c, c), lambda i: (0, i, 0, 0))
    return _call(
        body, name=name, grid=(n,), in_specs=[part(0), part(1), part(2), pl.BlockSpec((c, LANE), lambda i: (i, 0))],
        out_specs=[part(0), mat, mat],
        out_shape=[jax.ShapeDtypeStruct((t, WIDTH_A), F32), jax.ShapeDtypeStruct((nh, n, c, c), F32),
                   jax.ShapeDtypeStruct((nh, n, HEAD_A, HEAD_A), F32)],
        scratch_shapes=[pltpu.VMEM((nh, HEAD_A, HEAD_A), F32)], sem=("arbitrary",),
        args=(qkv, qkv, qkv, gates), comm=comm)


def _gdr_bwd(qkv, gates, tm_all, s_all, do, name, comm=None):
    t = qkv.shape[0]
    c, nh = GDR_CHUNK, N_HEADS_A
    n = t // c

    def body(q_ref, k_ref, v_ref, gb_ref, tm_ref, s_ref, do_ref, dqkv_ref, dgb_ref, dstate):
        @pl.when(pl.program_id(0) == 0)
        def _():
            dstate[...] = jnp.zeros_like(dstate)

        qv, kv, vv, dov = _heads_of(q_ref), _heads_of(k_ref), _heads_of(v_ref), _heads_of(do_ref)
        beta, g = _all_head_gates(gb_ref[...])
        tm, s, dsp = tm_ref[:, 0], s_ref[:, 0], dstate[...]
        row, col, causal, strict, gcum, decay, kb, kk = _gdr_terms(kv, beta, g)
        rowsum = lambda x: jnp.sum(x, axis=2, keepdims=True)
        e = jnp.exp(gcum)
        vb, kbe = vv * beta, kb * e
        u = _bdot(tm, vb, prec=HIGH)
        w = _bdot(tm, kbe, prec=HIGH)
        qk = _bdot(qv, kv, _B_NT)
        p = jnp.where(causal, qk * decay, 0.0)
        vn = u - _bdot(w, s)
        glast = gcum[:, c - 1:c, :]
        el = jnp.exp(glast)
        f = jnp.exp(glast - gcum)
        kd = kv * f
        qe = qv * e

        dvn = _bdot(p, dov, _B_TN) + _bdot(kd, dsp)
        dglast = el[:, :, 0:1] * jnp.sum(s * dsp, axis=(1, 2), keepdims=True)
        dkd = _bdot(vn, dsp, _B_NT)
        dk = dkd * f
        df = rowsum(dkd * kv) * f[:, :, 0:1]
        dglast = dglast + jnp.sum(df, axis=1, keepdims=True)
        dgc = -df
        dp = jnp.where(causal, _bdot(dov, vn, _B_NT), 0.0)
        dqe = _bdot(dov, s, _B_NT)
        dq = dqe * e
        de = rowsum(dqe * qv)
        dstate[...] = dsp * el + _bdot(qe, dov, _B_TN) - _bdot(w, dvn, _B_TN)
        dw = -_bdot(dvn, s, _B_NT)
        dvb = _bdot(tm, dvn, _B_TN, prec=HIGH)
        dkbe = _bdot(tm, dw, _B_TN, prec=HIGH)
        da = -jnp.where(strict, _bdot(dvb, u, _B_NT) + _bdot(dkbe, w, _B_NT), 0.0)
        dkk = da * decay
        dqk = dp * decay
        dd = da * kk + dp * qk
        dq = dq + _bdot(dqk, kv)
        dk = dk + _bdot(dqk, qv, _B_TN)
        dkb = _bdot(dkk, kv) + dkbe * e
        dk = dk + _bdot(dkk, kb, _B_TN)
        de = de + rowsum(dkbe * kb)
        dk = dk + dkb * beta
        dbeta = rowsum(dkb * kv) + rowsum(dvb * vv)
        m = dd * decay
        dgc = dgc + rowsum(m) - rowsum(jnp.swapaxes(m, 1, 2))
        dgc = dgc + de * e[:, :, 0:1]
        dgc = dgc + jnp.where(row[:, 0:1] == c - 1, dglast, 0.0)
        upper = jnp.broadcast_to((row <= col).astype(F32), (nh, c, c))
        dg = _bdot(upper, jnp.broadcast_to(dgc, (nh, c, c)), prec=HIGHEST)
        dv = dvb * beta
        for p, grad in enumerate((dq, dk, dv)):
            for h in range(nh):
                dqkv_ref[:, p * WIDTH_A + h * HEAD_A:p * WIDTH_A + (h + 1) * HEAD_A] = grad[h]
        head = lax.broadcasted_iota(jnp.int32, (nh, c, LANE), 0)
        lane = lax.broadcasted_iota(jnp.int32, (nh, c, LANE), 2)
        dgb_ref[...] = jnp.where(lane == head, dbeta, jnp.where(lane == head + nh, dg, 0.0))

    part = lambda p: pl.BlockSpec((c, WIDTH_A), lambda i: (n - 1 - i, p))
    mat = pl.BlockSpec((nh, 1, c, c), lambda i: (0, n - 1 - i, 0, 0))
    return _call(
        body, name=name, grid=(n,),
        in_specs=[part(0), part(1), part(2), pl.BlockSpec((c, LANE), lambda i: (n - 1 - i, 0)), mat, mat, part(0)],
        out_specs=[pl.BlockSpec((c, 3 * WIDTH_A), lambda i: (n - 1 - i, 0)),
                   pl.BlockSpec((nh, c, LANE), lambda i: (0, n - 1 - i, 0))],
        out_shape=[jax.ShapeDtypeStruct((t, 3 * WIDTH_A), F32), jax.ShapeDtypeStruct((nh, t, LANE), F32)],
        scratch_shapes=[pltpu.VMEM((nh, HEAD_A, HEAD_A), F32)], sem=("arbitrary",),
        args=(qkv, qkv, qkv, gates, tm_all, s_all, do), comm=comm)


def _onorm_fwd(o, gate, g, name):
    t = o.shape[0]

    def body(o_ref, gate_ref, g_ref, y_ref):
        ov, gv = o_ref[...], gate_ref[...]
        r = lax.rsqrt(jnp.mean(ov * ov, axis=-1, keepdims=True) + RMS_EPS)
        y_ref[...] = (ov * r * g_ref[...] * gv * _sigmoid(gv)).astype(BF16)

    blk = pl.BlockSpec((t, HEAD_A), lambda j: (0, j))
    return pl.pallas_call(
        body, name=name, grid=(N_HEADS_A,), in_specs=[blk, blk, pl.BlockSpec((1, HEAD_A), lambda j: (0, 0))],
        out_specs=blk, out_shape=jax.ShapeDtypeStruct((t, WIDTH_A), BF16),
        compiler_params=_params(("parallel",)))(o, gate, g)


def _onorm_bwd(o, gate, g, dy, name):
    t = o.shape[0]

    def body(o_ref, gate_ref, g_ref, dy_ref, do_ref, dgate_ref, dg_ref):
        @pl.when(pl.program_id(0) == 0)
        def _():
            dg_ref[...] = jnp.zeros_like(dg_ref)

        ov, gv, dyv = o_ref[...], gate_ref[...], dy_ref[...].astype(F32)
        r = lax.rsqrt(jnp.mean(ov * ov, axis=-1, keepdims=True) + RMS_EPS)
        oh = ov * r
        sg, dsg = _silu_and_grad(gv)
        dgate_ref[...] = (dyv * oh * g_ref[...] * dsg).astype(BF16)
        dn = dyv * sg
        dg_ref[...] += jnp.sum(dn * oh, axis=0, keepdims=True)
        dng = dn * g_ref[...]
        do_ref[...] = r * (dng - oh * jnp.mean(dng * oh, axis=-1, keepdims=True))

    blk = pl.BlockSpec((t, HEAD_A), lambda j: (0, j))
    vec = pl.BlockSpec((1, HEAD_A), lambda j: (0, 0))
    return pl.pallas_call(
        body, name=name, grid=(N_HEADS_A,), in_specs=[blk, blk, vec, blk], out_specs=[blk, blk, vec],
        out_shape=[jax.ShapeDtypeStruct((t, WIDTH_A), F32), jax.ShapeDtypeStruct((t, WIDTH_A), BF16),
                   jax.ShapeDtypeStruct((1, HEAD_A), F32)],
        compiler_params=_params(("arbitrary",)))(o, gate, g, dy)


def _cmul(ar, ai, br, bi):
    return ar * br - ai * bi, ar * bi + ai * br


def _scan_tables(ar, ai, reverse):
    p1 = (ar, ai)
    p2 = _cmul(*p1, *p1)
    p4 = _cmul(*p2, *p2)
    p8 = _cmul(*p4, *p4)
    p3 = _cmul(*p2, *p1)
    p5 = _cmul(*p4, *p1)
    p6 = _cmul(*p4, *p2)
    p7 = _cmul(*p4, *p3)
    pows = [p1, p2, p3, p4, p5, p6, p7, p8]
    rows = lax.broadcasted_iota(jnp.int32, (8, ar.shape[1]), 0)
    tr = jnp.zeros((8, ar.shape[1]), F32)
    ti = jnp.zeros((8, ar.shape[1]), F32)
    for r in range(8):
        pw = pows[7 - r] if reverse else pows[r]
        tr = jnp.where(rows == r, pw[0], tr)
        ti = jnp.where(rows == r, pw[1], ti)
    return p1, p2, p4, p8, tr, ti


def _tile_scan(xr, xi, p1, p2, p4, reverse):
    rows = lax.broadcasted_iota(jnp.int32, xr.shape, 0)
    for s, (pr, pi) in ((1, p1), (2, p2), (4, p4)):
        if reverse:
            keep = rows < 8 - s
            sr, si = pltpu.roll(xr, 8 - s, 0), pltpu.roll(xi, 8 - s, 0)
        else:
            keep = rows >= s
            sr, si = pltpu.roll(xr, s, 0), pltpu.roll(xi, s, 0)
        sr, si = jnp.where(keep, sr, 0.0), jnp.where(keep, si, 0.0)
        mr, mi = _cmul(pr, pi, sr, si)
        xr, xi = xr + mr, xi + mi
    return xr, xi


def _s5_scan_fwd(bu, a, name, tb=512, comm=None):
    t = bu.shape[0]
    cb = SCAN_CB
    nt = t // tb

    def body(b_ref, a_ref, x_ref, carry):
        @pl.when(pl.program_id(1) == 0)
        def _():
            carry[...] = jnp.zeros_like(carry)

        ar, ai = a_ref[:, 0:cb], a_ref[:, cb:2 * cb]
        p1, p2, p4, p8, tr, ti = _scan_tables(ar, ai, False)

        def step(j, c):
            cr, ci = c
            i = pl.multiple_of(j * 8, 8)
            xr, xi = _tile_scan(b_ref[pl.ds(i, 8), 0:cb], b_ref[pl.ds(i, 8), cb:2 * cb], p1, p2, p4, False)
            mr, mi = _cmul(tr, ti, cr, ci)
            xr, xi = xr + mr, xi + mi
            x_ref[pl.ds(i, 8), 0:cb] = xr
            x_ref[pl.ds(i, 8), cb:2 * cb] = xi
            return xr[7:8, :], xi[7:8, :]

        cr, ci = lax.fori_loop(0, tb // 8, step, (carry[0:1, :], carry[1:2, :]), unroll=2)
        carry[0:1, :] = cr
        carry[1:2, :] = ci

    blk = pl.BlockSpec((tb, 2 * cb), lambda j, i: (i, j))
    return _call(
        body, name=name, grid=(SSM_CH // cb, nt),
        in_specs=[blk, pl.BlockSpec((1, 2 * cb), lambda j, i: (0, j))], out_specs=blk,
        out_shape=jax.ShapeDtypeStruct((t, 2 * SSM_CH), F32), scratch_shapes=[pltpu.VMEM((8, cb), F32)],
        sem=("parallel", "arbitrary"), args=(bu, a), comm=comm)


def _s5_scan_bwd(dx, x, a, name, tb=512, comm=None):
    t = dx.shape[0]
    cb = SCAN_CB
    nt = t // tb
    nj = tb // 8

    def body(d_ref, x_ref, xp_ref, a_ref, l_ref, da_ref, carry, acc):
        tblk = pl.program_id(1)

        @pl.when(tblk == 0)
        def _():
            carry[...] = jnp.zeros_like(carry)
            acc[...] = jnp.zeros_like(acc)

        ar, ai = a_ref[:, 0:cb], a_ref[:, cb:2 * cb]
        p1, p2, p4, p8, tr, ti = _scan_tables(ar, -ai, True)
        rows = lax.broadcasted_iota(jnp.int32, (8, cb), 0)

        def step(jj, c):
            cr, ci, sr_acc, si_acc = c
            j = nj - 1 - jj
            i = pl.multiple_of(j * 8, 8)
            lr, li = _tile_scan(d_ref[pl.ds(i, 8), 0:cb], d_ref[pl.ds(i, 8), cb:2 * cb], p1, p2, p4, True)
            mr, mi = _cmul(tr, ti, cr, ci)
            lr, li = lr + mr, li + mi
            l_ref[pl.ds(i, 8), 0:cb] = lr
            l_ref[pl.ds(i, 8), cb:2 * cb] = li
            ip = pl.multiple_of(jnp.maximum(j - 1, 0) * 8, 8)
            prev_r = jnp.where(j > 0, x_ref[pl.ds(ip, 8), 0:cb], xp_ref[:, 0:cb])
            prev_i = jnp.where(j > 0, x_ref[pl.ds(ip, 8), cb:2 * cb], xp_ref[:, cb:2 * cb])
            edge = jnp.where(jnp.logical_and(j == 0, tblk == nt - 1), 0.0, 1.0)
            xs_r = jnp.where(rows == 0, pltpu.roll(prev_r, 1, 0) * edge, pltpu.roll(x_ref[pl.ds(i, 8), 0:cb], 1, 0))
            xs_i = jnp.where(rows == 0, pltpu.roll(prev_i, 1, 0) * edge, pltpu.roll(x_ref[pl.ds(i, 8), cb:2 * cb], 1, 0))
            sr_acc = sr_acc + lr * xs_r + li * xs_i
            si_acc = si_acc + li * xs_r - lr * xs_i
            return lr[0:1, :], li[0:1, :], sr_acc, si_acc

        cr, ci, sr_acc, si_acc = lax.fori_loop(
            0, nj, step, (carry[0:1, :], carry[1:2, :], acc[:, 0:cb], acc[:, cb:2 * cb]))
        carry[0:1, :] = cr
        carry[1:2, :] = ci
        acc[:, 0:cb] = sr_acc
        acc[:, cb:2 * cb] = si_acc

        @pl.when(tblk == nt - 1)
        def _():
            da_ref[...] = jnp.sum(acc[...], axis=0, keepdims=True)

    blk = pl.BlockSpec((tb, 2 * cb), lambda j, i: (nt - 1 - i, j))
    prev = pl.BlockSpec((8, 2 * cb), lambda j, i: (jnp.maximum((nt - 1 - i) * (tb // 8) - 1, 0), j))
    vec = pl.BlockSpec((1, 2 * cb), lambda j, i: (0, j))
    return _call(
        body, name=name, grid=(SSM_CH // cb, nt), in_specs=[blk, blk, prev, vec], out_specs=[blk, vec],
        out_shape=[jax.ShapeDtypeStruct((t, 2 * SSM_CH), F32), jax.ShapeDtypeStruct((1, 2 * SSM_CH), F32)],
        scratch_shapes=[pltpu.VMEM((8, cb), F32), pltpu.VMEM((8, 2 * cb), F32)],
        sem=("parallel", "arbitrary"), args=(dx, x, x, a), comm=comm)


def _glu_fwd(yc, u, dvec, wg, bg, name, tr=256):
    t = yc.shape[0]

    def body(yc_ref, u_ref, d_ref, w_ref, b_ref, yl_ref, yb_ref):
        yl = yc_ref[...] + d_ref[...] * u_ref[...]
        yl_ref[...] = yl
        yg, _ = _gelu_and_grad(yl)
        z = jnp.dot(yg.astype(BF16), w_ref[...], preferred_element_type=F32) + b_ref[...]
        yb_ref[...] = (yg * _sigmoid(z)).astype(BF16)

    blk = pl.BlockSpec((tr, SSM_WIDTH), lambda i: (i, 0))
    vec = pl.BlockSpec((1, SSM_WIDTH), lambda i: (0, 0))
    return pl.pallas_call(
        body, name=name, grid=(t // tr,),
        in_specs=[blk, blk, vec, pl.BlockSpec((SSM_WIDTH, SSM_WIDTH), lambda i: (0, 0)), vec],
        out_specs=[blk, blk],
        out_shape=[jax.ShapeDtypeStruct((t, SSM_WIDTH), F32), jax.ShapeDtypeStruct((t, SSM_WIDTH), BF16)],
        compiler_params=_params(("parallel",)))(yc, u, dvec, wg, bg)


def _glu_bwd(yl, u, dvec, wg, bg, dyb, name, tr=256):
    t = yl.shape[0]

    def body(yl_ref, u_ref, d_ref, w_ref, b_ref, dy_ref, dyl_ref, du_ref, dw_ref, db_ref, dd_ref):
        @pl.when(pl.program_id(0) == 0)
        def _():
            dw_ref[...] = jnp.zeros_like(dw_ref)
            db_ref[...] = jnp.zeros_like(db_ref)
            dd_ref[...] = jnp.zeros_like(dd_ref)

        ylv, dyv, wv = yl_ref[...], dy_ref[...].astype(F32), w_ref[...]
        yg, dgelu = _gelu_and_grad(ylv)
        ygb = yg.astype(BF16)
        z = jnp.dot(ygb, wv, preferred_element_type=F32) + b_ref[...]
        sg = _sigmoid(z)
        dz = dyv * yg * sg * (1.0 - sg)
        dzb = dz.astype(BF16)
        dyg = dyv * sg + lax.dot_general(dzb, wv, (((1,), (1,)), ((), ())), preferred_element_type=F32)
        dyl = dyg * dgelu
        dyl_ref[...] = dyl.astype(BF16)
        du_ref[...] = dyl * d_ref[...]
        dw_ref[...] += lax.dot_general(ygb, dzb, (((0,), (0,)), ((), ())), preferred_element_type=F32)
        db_ref[...] += jnp.sum(dz, axis=0, keepdims=True)
        dd_ref[...] += jnp.sum(dyl * u_ref[...], axis=0, keepdims=True)

    blk = pl.BlockSpec((tr, SSM_WIDTH), lambda i: (i, 0))
    vec = pl.BlockSpec((1, SSM_WIDTH), lambda i: (0, 0))
    wsp = pl.BlockSpec((SSM_WIDTH, SSM_WIDTH), lambda i: (0, 0))
    return pl.pallas_call(
        body, name=name, grid=(t // tr,), in_specs=[blk, blk, vec, wsp, vec, blk],
        out_specs=[blk, blk, wsp, vec, vec],
        out_shape=[jax.ShapeDtypeStruct((t, SSM_WIDTH), BF16), jax.ShapeDtypeStruct((t, SSM_WIDTH), F32),
                   jax.ShapeDtypeStruct((SSM_WIDTH, SSM_WIDTH), F32), jax.ShapeDtypeStruct((1, SSM_WIDTH), F32),
                   jax.ShapeDtypeStruct((1, SSM_WIDTH), F32)],
        compiler_params=_params(("arbitrary",)))(yl, u, dvec, wg, bg, dyb)


def _mesh_pos():
    return lax.axis_index("x"), lax.axis_index("y"), lax.axis_index("c")


def _device_index():
    x, y, c = _mesh_pos()
    return 4 * x + 2 * y + c


def _gather_comm(arrays):
    na = len(arrays)

    def own_copy(ins, outs, sems, ai):
        return pltpu.make_async_copy(ins[ai], outs[ai].at[_device_index()], sems[2].at[ai])

    def ctx(ins, outs, sems):
        send_sems, recv_sems = sems[:2]
        x, y, c = _mesh_pos()
        chips = [(1 - x, y), (x, 1 - y), (1 - x, 1 - y)]

        def copy(ai, kk, block, to, own=False):
            slot = outs[ai].at[4 * block[0] + 2 * block[1] + block[2]]
            return pltpu.make_async_remote_copy(
                src_ref=ins[ai] if own else slot, dst_ref=slot, send_sem=send_sems.at[ai, kk],
                recv_sem=recv_sems.at[ai, kk], device_id=to, device_id_type=MESH)

        return (x, y, c), (x, y, 1 - c), chips, c, copy

    def start(ins, outs, sems):
        me, sibling, chips, c, copy = ctx(ins, outs, sems)
        for ai in range(na):
            copy(ai, 0, me, sibling, own=True).start()
            for j, chip in enumerate(chips):
                copy(ai, 1 + j, me, (*chip, c), own=True).start()
        for ai in range(na):
            own_copy(ins, outs, sems, ai).start()

    def mid(ins, outs, sems):
        me, sibling, chips, c, copy = ctx(ins, outs, sems)
        for ai in range(na):
            for j, chip in enumerate(chips):
                copy(ai, 1 + j, (*chip, c), me).wait_recv()
                copy(ai, 4 + j, (*chip, c), sibling).start()

    def end(ins, outs, sems):
        me, sibling, chips, c, copy = ctx(ins, outs, sems)
        for ai in range(na):
            copy(ai, 0, sibling, me).wait_recv()
            copy(ai, 0, me, sibling, own=True).wait_send()
            for j, chip in enumerate(chips):
                copy(ai, 4 + j, (*chip, 1 - c), me).wait_recv()
                copy(ai, 1 + j, me, (*chip, c), own=True).wait_send()
                copy(ai, 4 + j, (*chip, c), sibling).wait_send()
            own_copy(ins, outs, sems, ai).wait()

    return Comm(arrays, [jax.ShapeDtypeStruct((N_DEV,) + a.shape, a.dtype) for a in arrays],
                [pltpu.SemaphoreType.DMA((na, 7)), pltpu.SemaphoreType.DMA((na, 7)), pltpu.SemaphoreType.DMA((na,))],
                start, end, mid)


def _sequencer_gather(arrays, name, collective_id):
    comm = _gather_comm(arrays)
    na = len(arrays)

    def body(*refs):
        ins, outs, sems = refs[:na], refs[na:2 * na], refs[2 * na:]
        x, y, c = _mesh_pos()
        peers = [(x, y, 1 - c), (1 - x, y, c), (x, 1 - y, c), (1 - x, 1 - y, c)]
        barrier = pltpu.get_barrier_semaphore()
        for peer in peers:
            pl.semaphore_signal(barrier, inc=1, device_id=peer, device_id_type=MESH)
        pl.semaphore_wait(barrier, len(peers))
        comm.start(ins, outs, sems)
        comm.mid(ins, outs, sems)
        comm.end(ins, outs, sems)

    return list(pl.kernel(
        body, out_type=tuple(comm.out_shapes), mesh=plsc.ScalarSubcoreMesh(axis_name="sequencer", num_cores=1),
        name=name, scratch_types=tuple(comm.sems),
        compiler_params=pltpu.CompilerParams(collective_id=collective_id))(*arrays))


def _sequencer_exchange(comm, peers_of, name, collective_id):
    na = len(comm.inputs)

    def body(*refs):
        ins, outs, sems = refs[:na], refs[na:na + len(comm.out_shapes)], refs[na + len(comm.out_shapes):]
        peers = peers_of(*_mesh_pos())
        barrier = pltpu.get_barrier_semaphore()
        for peer in peers:
            pl.semaphore_signal(barrier, inc=1, device_id=peer, device_id_type=MESH)
        pl.semaphore_wait(barrier, len(peers))
        comm.start(ins, outs, sems)
        comm.end(ins, outs, sems)

    return list(pl.kernel(
        body, out_type=tuple(comm.out_shapes), mesh=plsc.ScalarSubcoreMesh(axis_name="sequencer", num_cores=1),
        name=name, scratch_types=tuple(comm.sems),
        compiler_params=pltpu.CompilerParams(collective_id=collective_id))(*comm.inputs))


SIBLING_SWAP_ID, CHIP_EXCHANGE_ID = 9, 10


def _sequencer_swap(arrays, name):
    return _sequencer_exchange(_swap_comm(arrays), lambda x, y, c: [(x, y, 1 - c)], name, SIBLING_SWAP_ID)[0]


def _sequencer_chips(send, name):
    return _sequencer_exchange(_chips_comm(send), lambda x, y, c: [(1 - x, y, c), (x, 1 - y, c), (1 - x, 1 - y, c)],
                               name, CHIP_EXCHANGE_ID)[0]


def _swap_comm(arrays):
    na = len(arrays)
    offs = np.concatenate([[0], np.cumsum([a.shape[1] for a in arrays])]).astype(int)

    def copies(ins, outs, sems):
        x, y, c = _mesh_pos()
        return [pltpu.make_async_remote_copy(
            src_ref=ins[ai].at[2 * k + 1 - c], dst_ref=outs[0].at[k, pl.ds(int(offs[ai]), arrays[ai].shape[1])],
            send_sem=sems[0].at[ai, k], recv_sem=sems[1].at[ai, k], device_id=(x, y, 1 - c), device_id_type=MESH)
            for ai in range(na) for k in range(4)]

    def start(ins, outs, sems):
        for cp in copies(ins, outs, sems):
            cp.start()

    def end(ins, outs, sems):
        for cp in copies(ins, outs, sems):
            cp.wait()

    return Comm(arrays, [jax.ShapeDtypeStruct((4, int(offs[-1]), PACK_COLS), arrays[0].dtype)],
                [pltpu.SemaphoreType.DMA((na, 4)), pltpu.SemaphoreType.DMA((na, 4))], start, end)


def _chips_comm(send):
    def copies(ins, outs, sems):
        x, y, c = _mesh_pos()
        chips = [(1 - x, y), (x, 1 - y), (1 - x, 1 - y)]
        return [pltpu.make_async_remote_copy(
            src_ref=ins[0].at[2 * cx + cy], dst_ref=outs[0].at[j], send_sem=sems[0].at[j], recv_sem=sems[1].at[j],
            device_id=(cx, cy, c), device_id_type=MESH) for j, (cx, cy) in enumerate(chips)]

    def start(ins, outs, sems):
        for cp in copies(ins, outs, sems):
            cp.start()

    def end(ins, outs, sems):
        for cp in copies(ins, outs, sems):
            cp.wait()

    return Comm([send], [jax.ShapeDtypeStruct((3,) + send.shape[1:], send.dtype)],
                [pltpu.SemaphoreType.DMA((3,)), pltpu.SemaphoreType.DMA((3,))], start, end)


def _pair_sum(keep, recv, name, tr=464):
    nchip, rows, cols = keep.shape

    def body(g_ref, r_ref, o_ref):
        o_ref[...] = (g_ref[...].astype(F32) + r_ref[...].astype(F32)).astype(BF16)

    blk = pl.BlockSpec((1, tr, cols), lambda k, i: (k, i, 0))
    return pl.pallas_call(
        body, name=name, grid=(nchip, rows // tr), in_specs=[blk, blk], out_specs=blk,
        out_shape=jax.ShapeDtypeStruct((nchip, rows, cols), BF16),
        compiler_params=_params(("parallel", "parallel")))(keep, recv)


def _pair_sum_pieces(pieces, recv, name, tr):
    _, rows, cols = pieces.shape
    core = lax.axis_index("c").astype(jnp.int32).reshape(1)

    def body(c_ref, g_ref, r_ref, o_ref):
        del c_ref
        o_ref[...] = (g_ref[...].astype(F32) + r_ref[...].astype(F32)).astype(BF16)

    grid_spec = pltpu.PrefetchScalarGridSpec(
        num_scalar_prefetch=1, grid=(4, rows // tr),
        in_specs=[pl.BlockSpec((1, tr, cols), lambda k, i, c_ref: (2 * k + c_ref[0], i, 0)),
                  pl.BlockSpec((1, tr, cols), lambda k, i, c_ref: (k, i, 0))],
        out_specs=pl.BlockSpec((1, tr, cols), lambda k, i, c_ref: (k, i, 0)))
    return pl.pallas_call(
        body, name=name, grid_spec=grid_spec, out_shape=jax.ShapeDtypeStruct((4, rows, cols), BF16),
        compiler_params=_params(("parallel", "parallel")))(core, pieces, recv)


def _chip_sum(own, others, name, tr=464):
    _, rows, cols = own.shape
    chip = (2 * lax.axis_index("x") + lax.axis_index("y")).astype(jnp.int32).reshape(1)

    def body(chip_ref, own_ref, oth_ref, o_ref):
        del chip_ref
        acc = own_ref[0].astype(F32)
        for j in range(3):
            acc = acc + oth_ref[j].astype(F32)
        o_ref[...] = acc

    grid_spec = pltpu.PrefetchScalarGridSpec(
        num_scalar_prefetch=1, grid=(rows // tr,),
        in_specs=[pl.BlockSpec((1, tr, cols), lambda i, chip_ref: (chip_ref[0], i, 0)),
                  pl.BlockSpec((3, tr, cols), lambda i, chip_ref: (0, i, 0))],
        out_specs=pl.BlockSpec((tr, cols), lambda i, chip_ref: (i, 0)))
    return pl.pallas_call(
        body, name=name, grid_spec=grid_spec, out_shape=jax.ShapeDtypeStruct((rows, cols), F32),
        compiler_params=_params(("parallel",)))(chip, own, others)


def _sum_leading(parts, name, tr=464):
    nparts, rows, cols = parts.shape
    tr = tr if rows % tr == 0 else rows

    def body(p_ref, o_ref):
        acc = p_ref[0].astype(F32)
        for i in range(1, nparts):
            acc = acc + p_ref[i].astype(F32)
        o_ref[...] = acc

    return pl.pallas_call(
        body, name=name, grid=(rows // tr,),
        in_specs=[pl.BlockSpec((nparts, tr, cols), lambda i: (0, i, 0))],
        out_specs=pl.BlockSpec((tr, cols), lambda i: (i, 0)), out_shape=jax.ShapeDtypeStruct((rows, cols), F32),
        compiler_params=_params(("parallel",)))(parts)


def _adamw(w, g, m, v, name, comm=None):
    shape = w.shape
    cols = shape[-1]
    lead = shape[0] if len(shape) >= 3 else 1
    rows = int(np.prod(shape[:-1])) // lead if len(shape) > 1 else 1
    w2, g2, m2, v2 = (a.reshape(lead, rows, cols) for a in (w, g, m, v))
    tr = rows
    for cand in (512, 256, 128, 64, 32, 16, 8):
        if rows % cand == 0 and rows > cand:
            tr = cand
            break
    bc1, bc2 = 1.0 - ADAM_B1 ** ADAM_STEP, 1.0 - ADAM_B2 ** ADAM_STEP

    def body(w_ref, g_ref, m_ref, v_ref, d_ref, nm_ref, nv_ref):
        gv = g_ref[...]
        nm = ADAM_B1 * m_ref[...] + (1.0 - ADAM_B1) * gv
        nv = ADAM_B2 * v_ref[...] + (1.0 - ADAM_B2) * (gv * gv)
        nm_ref[...] = nm
        nv_ref[...] = nv
        d_ref[...] = -ADAM_LR * ((nm / bc1) / (jnp.sqrt(nv / bc2) + ADAM_EPS) + ADAM_WD * w_ref[...])

    blk = pl.BlockSpec((1, tr, cols), lambda l, i: (l, i, 0))
    res = _call(body, name=name, grid=(lead, rows // tr), in_specs=[blk] * 4, out_specs=[blk] * 3,
                out_shape=[jax.ShapeDtypeStruct((lead, rows, cols), F32)] * 3, sem=("parallel", "parallel"),
                args=(w2, g2, m2, v2), comm=comm)
    outs, couts = res if comm is not None else (res, None)
    outs = tuple(o.reshape(shape) for o in outs)
    return outs if comm is None else (outs, couts)


WEIGHT_NAMES = ['norm_mix_g', 'norm_xa_g', 'norm_ffn_g', 'norm_mem_g', 'norm_final_g', 'w_in_ab', 'conv_qkv_a',
                'a_log_a', 'dt_bias_a', 'onorm_g_a', 'ssm_lambda_re', 'ssm_lambda_im', 'ssm_b_re', 'ssm_b_im',
                'ssm_c_re', 'ssm_c_im', 'ssm_d', 'ssm_log_dt', 'w_glu_b', 'b_glu_b', 'w_out_ab', 'pool_w',
                'pool_scale', 'xa_wq', 'xa_wkv', 'xa_wo', 'ffn_w_up', 'ffn_conv', 'ffn_w_down']
BIG_SHARDED = {'w_in_ab': ((1, 1024, 2568), 2), 'w_glu_b': ((1, 512, 512), 1), 'w_out_ab': ((1, 1024, 1024), 1),
               'pool_w': ((1, 4, 256, 256), 2), 'xa_wq': ((2, 1024, 1024), 1), 'xa_wkv': ((2, 1024, 2048), 2),
               'xa_wo': ((2, 1024, 1024), 1), 'ffn_w_up': ((2, 1024, 5632), 2), 'ffn_w_down': ((2, 2816, 1024), 1)}
SMALL_SHARDED = {'conv_qkv_a': ((1, 4, 1536), 2), 'pool_scale': ((1, 1024), 1), 'ffn_conv': ((2, 3, 5632), 2)}
REPLICATED = {'norm_mix_g': (2, 1024), 'norm_xa_g': (2, 1024), 'norm_ffn_g': (2, 1024), 'norm_mem_g': (1024,),
              'norm_final_g': (1024,), 'a_log_a': (1, 4), 'dt_bias_a': (1, 4), 'onorm_g_a': (1, 128),
              'ssm_lambda_re': (1, 32, 64), 'ssm_lambda_im': (1, 32, 64), 'ssm_b_re': (1, 32, 64, 16),
              'ssm_b_im': (1, 32, 64, 16), 'ssm_c_re': (1, 32, 16, 64), 'ssm_c_im': (1, 32, 16, 64),
              'ssm_d': (1, 32, 16), 'ssm_log_dt': (1, 32), 'b_glu_b': (1, 512)}
PACK_ROW_ALIGN = 8


def _shard_shape(shape, axis):
    return tuple(s // N_DEV if i == axis else s for i, s in enumerate(shape))


def _round_up(n, m):
    return (n + m - 1) // m * m


def _pack(arrays):
    total = sum(int(np.prod(a.shape)) for a in arrays)
    padded = _round_up(total, PACK_COLS * PACK_ROW_ALIGN)
    parts = [a.astype(F32).reshape(-1) for a in arrays]
    if padded != total:
        parts.append(jnp.zeros((padded - total,), F32))
    return jnp.concatenate(parts).reshape(padded // PACK_COLS, PACK_COLS)


def _unpack(packed, shapes):
    flat, out, off = packed.reshape(-1), [], 0
    for shape in shapes:
        size = int(np.prod(shape))
        out.append(flat[off:off + size].reshape(shape))
        off += size
    return out


def _split_shards(full, axis):
    shape = full.shape
    s = shape[axis] // N_DEV
    a = full.reshape(shape[:axis] + (N_DEV, s) + shape[axis + 1:])
    return jnp.moveaxis(a, axis, 0).reshape(N_DEV, -1)


def _merge_shards(pieces, shape, axis):
    sh = _shard_shape(shape, axis)
    a = pieces.reshape((N_DEV,) + sh)
    a = jnp.moveaxis(a, 0, axis)
    return a.reshape(shape)


_SCAN_NB = SSM_CH // SCAN_CB


def _to_scan_layout(m, axis):
    shape = m.shape
    m = m.reshape(shape[:axis] + (2, _SCAN_NB, SCAN_CB) + shape[axis + 1:])
    return jnp.swapaxes(m, axis, axis + 1).reshape(shape)


def _from_scan_layout(m, axis):
    shape = m.shape
    m = m.reshape(shape[:axis] + (_SCAN_NB, 2, SCAN_CB) + shape[axis + 1:])
    return jnp.swapaxes(m, axis, axis + 1).reshape(shape)


def _s5_discretise(lam_re, lam_im, b_re, b_im, log_dt):
    dt = jnp.exp(log_dt)[:, None]
    mag = jnp.exp(lam_re * dt)
    ang = lam_im * dt
    lb_re, lb_im = mag * jnp.cos(ang), mag * jnp.sin(ang)
    den = lam_re * lam_re + lam_im * lam_im
    nr, ni = lb_re - 1.0, lb_im
    coef_re = (nr * lam_re + ni * lam_im) / den
    coef_im = (ni * lam_re - nr * lam_im) / den
    bb_re = coef_re[..., None] * b_re - coef_im[..., None] * b_im
    bb_im = coef_re[..., None] * b_im + coef_im[..., None] * b_re
    return lb_re, lb_im, bb_re, bb_im


_GROUPS_PER_BLOCK = N_GROUPS // _SCAN_NB
_U_BLOCK = _GROUPS_PER_BLOCK * SSM_GROUP


def _s5_matrices(lb_re, lb_im, bb_re, bb_im, c_re, c_im):
    eye = jnp.eye(_GROUPS_PER_BLOCK, dtype=F32)
    blocked = lambda m: m.reshape((_SCAN_NB, _GROUPS_PER_BLOCK) + m.shape[1:])
    bmat = lambda bb: jnp.einsum('jgph,gk->jghkp', blocked(bb), eye).reshape(_SCAN_NB, _U_BLOCK, SCAN_CB)
    cmat = lambda cc: jnp.einsum('jghp,gk->jkpgh', blocked(cc), eye).reshape(_SCAN_NB, SCAN_CB, _U_BLOCK)
    b_in = jnp.concatenate([bmat(bb_re), bmat(bb_im)], axis=2)
    c_out = jnp.concatenate([cmat(c_re), -cmat(c_im)], axis=1)
    a_row = _to_scan_layout(jnp.concatenate([lb_re.reshape(1, SSM_CH), lb_im.reshape(1, SSM_CH)], axis=1), 1)
    return b_in, c_out, a_row


def _s5_matrix_grads(db_in, dc_out, da_row):
    da_nat = _from_scan_layout(da_row, 1)
    eye = jnp.eye(_GROUPS_PER_BLOCK, dtype=F32)
    nb, gb = _SCAN_NB, _GROUPS_PER_BLOCK
    bgrad = lambda m: jnp.einsum('jghkp,gk->jgph', m.reshape(nb, gb, SSM_GROUP, gb, SSM_STATE), eye
                                 ).reshape(N_GROUPS, SSM_STATE, SSM_GROUP)
    cgrad = lambda m: jnp.einsum('jkpgh,gk->jghp', m.reshape(nb, gb, SSM_STATE, gb, SSM_GROUP), eye
                                 ).reshape(N_GROUPS, SSM_GROUP, SSM_STATE)
    dbb_re, dbb_im = bgrad(db_in[:, :, :SCAN_CB]), bgrad(db_in[:, :, SCAN_CB:])
    dc_re, dc_im = cgrad(dc_out[:, :SCAN_CB]), -cgrad(dc_out[:, SCAN_CB:])
    dlb_re = da_nat[0, :SSM_CH].reshape(N_GROUPS, SSM_STATE)
    dlb_im = da_nat[0, SSM_CH:].reshape(N_GROUPS, SSM_STATE)
    return dlb_re, dlb_im, dbb_re, dbb_im, dc_re, dc_im


def _as_pieces(a):
    return a.reshape(N_DEV, a.shape[0] // N_DEV, a.shape[1])


def _hybrid_fwd(xn, x, wts, p, weights, riders):
    sv = {}
    hq = _mm(xn, wts['w_qkv_t'], "nt", "l0_in_qkv")
    gate = _mm(xn, wts['w_gate_t'], "nt", "l0_in_gate")
    ba = _mm(xn, wts['w_ba_t'], "nt", "l0_in_ba")
    u = _mm(xn, wts['w_u_t'], "nt", "l0_in_u")
    conv = p['conv_qkv']
    qkv = _qkv_pre_fwd(hq, conv, "l0_qkv_pre")
    gates = _gates_fwd(ba, p['arow'], p['brow'], "l0_gates")
    o, tm_all, s_all = riders.run("l0_gdr_fwd", _gdr_fwd, qkv, gates)
    wts['w_glu'], wts['w_out'] = weights.full['w_glu'], weights.full['w_out']
    y_a = _onorm_fwd(o, gate, p['onorm_g'], "l0_onorm")
    bu = riders.run("l0_s5_bu", _mm_bd, u, p['b_in'], "nn")
    xs = riders.run("l0_s5_scan", _s5_scan_fwd, bu, p['a_row'])
    weights.gather_by_sequencer(GATHER_LAYER1, xs, "gather_layer1", GATHER_LAYER1_ID)
    yc = riders.run("l0_s5_cx", _mm_bd, xs, p['c_out'], "nn")
    yl, y_b = _glu_fwd(yc, u, p['d_row'], wts['w_glu'], p['b_glu'], "l0_glu")
    mixed = jnp.concatenate([y_a, y_b], axis=1)
    x1 = _mm(mixed, wts['w_out'], "nn", "l0_out", res=x)
    sv.update(hq=hq, gate=gate, ba=ba, u=u, qkv=qkv, gb=gates, o=o, tm=tm_all, s=s_all, xs=xs, yl=yl, mixed=mixed)
    return x1, sv


def _hybrid_bwd(dx1, xn, wts, p, sv, riders):
    gr = {}
    dmixed = _mm(dx1, wts['w_out'], "nt", "l0_out_dx", out_dtype=BF16)
    riders.grad('w_out', _as_pieces(_mm(sv['mixed'], dx1, "tn", "l0_out_dw", out_dtype=BF16)))
    dya, dyb = dmixed[:, :WIDTH_A], dmixed[:, WIDTH_A:]
    dyl, du_direct, dw_glu, gr['b_glu_b'], dd = _glu_bwd(
        sv['yl'], sv['u'], p['d_row'], wts['w_glu'], p['b_glu'], dyb, "l0_glu_bwd")
    riders.grad('w_glu', dw_glu.astype(BF16).reshape(N_DEV, -1, PACK_COLS))
    dxs = riders.run("l0_s5_cx_dx", _mm_bd, dyl, p['c_out'], "nt")
    dc_out = _mm_bd(sv['xs'], dyl, "tn", "l0_s5_cx_dw")
    lam, da_row = riders.run("l0_s5_scan_bwd", _s5_scan_bwd, dxs, sv['xs'], p['a_row'])
    du = _mm_bd(lam, p['b_in'], "nt", "l0_s5_bu_dx", res=du_direct, out_dtype=BF16)
    db_in = _mm_bd(sv['u'], lam, "tn", "l0_s5_bu_dw")
    gr['s5'] = (db_in, dc_out, da_row, dd)
    do, dgate, gr['onorm_g_a'] = _onorm_bwd(sv['o'], sv['gate'], p['onorm_g'], dya, "l0_onorm_bwd")
    dqkv, dgb = riders.run("l0_gdr_bwd", _gdr_bwd, sv['qkv'], sv['gb'], sv['tm'], sv['s'], do)
    dhq, gr['conv_qkv_a'] = _qkv_pre_bwd(sv['hq'], p['conv_qkv'], dqkv, "l0_qkv_pre_bwd")
    dba, da_log, ddt_bias = _gates_bwd(sv['ba'], p['arow'], p['brow'], dgb, "l0_gates_bwd")
    gr['a_log_a'], gr['dt_bias_a'] = da_log[:, 4:8], ddt_bias[:, 4:8]
    dw_qkv_t = _mm(dhq, xn, "tn", "l0_in_qkv_dw", out_dtype=BF16)
    dw_gate_t = _mm(dgate, xn, "tn", "l0_in_gate_dw", out_dtype=BF16)
    dw_ba_t = _mm(dba, xn, "tn", "l0_in_ba_dw", out_dtype=BF16)
    dw_u_t = _mm(du, xn, "tn", "l0_in_u_dw", out_dtype=BF16)
    dw_in_t = _as_pieces(jnp.concatenate([dw_qkv_t, dw_gate_t, dw_ba_t[:8], dw_u_t], axis=0))
    riders.grad('w_in_t', jnp.concatenate(
        [dw_in_t, jnp.zeros((N_DEV, dict(PIECES)['w_in_t'] - W_IN_PIECE, D_MODEL), BF16)], axis=1))
    dxn = riders.run("l0_in_qkv_dx", _mm, dhq, wts['w_qkv_t'], "nn")
    dxn = _mm(dgate, wts['w_gate_t'], "nn", "l0_in_gate_dx", res=dxn)
    dxn = _mm(dba, wts['w_ba_t'], "nn", "l0_in_ba_dx", res=dxn)
    dxn = riders.run("l0_in_u_dx", _mm, du, wts['w_u_t'], "nn", res=dxn)
    return dxn, gr


def _xa_fwd(x1, g, mem_n, wq, wkv_t, wo, tag, riders):
    xq = _rms_fwd(x1, g, BF16, tag + "_norm")
    q = _mm(xq, wq, "nn", tag + "_q", out_dtype=BF16)
    kv = _mm(mem_n, wkv_t, "nt", tag + "_kv", out_dtype=BF16)
    o = riders.run(tag + "_attn", _attn_fwd, q, kv)
    x2 = _mm(o, wo, "nn", tag + "_o", res=x1)
    return x2, dict(xq=xq, q=q, kv=kv, o=o)


def _xa_bwd(dx2, x1, g, mem_n, wq, wkv_t, wo, sv, tag, layer, riders):
    do = _mm(dx2, wo, "nt", tag + "_o_dx", out_dtype=BF16)
    riders.grad('wo%d' % layer, _as_pieces(_mm(sv['o'], dx2, "tn", tag + "_o_dw", out_dtype=BF16)))
    dq, dk, dv = _attn_bwd(sv['q'], sv['kv'], do, tag + "_attn_bwd")
    dkv = jnp.concatenate([dk, dv], axis=1).astype(BF16)
    dxq = _mm(dq, wq, "nt", tag + "_q_dx")
    riders.grad('wq%d' % layer, _as_pieces(_mm(sv['xq'], dq, "tn", tag + "_q_dw", out_dtype=BF16)))
    dmem_n = _mm(dkv, wkv_t, "nn", tag + "_kv_dx")
    riders.grad('wkv_t%d' % layer, _as_pieces(_mm(dkv, mem_n, "tn", tag + "_kv_dw", out_dtype=BF16)))
    dx1, dg = riders.run(tag + "_norm_bwd", _rms_bwd, x1, g, dxq, dx2)
    return dx1, dmem_n, dg


def _ffn_fwd(x2, g, w_up_t, conv, w_down, tag, riders):
    xf = _rms_fwd(x2, g, BF16, tag + "_norm")
    h = riders.run(tag + "_up", _mm, xf, w_up_t, "nt")
    a = riders.run(tag + "_act", _ffn_act_fwd, h, conv)
    x3 = _mm(a, w_down, "nn", tag + "_down", res=x2)
    return x3, dict(xf=xf, h=h, a=a)


def _ffn_bwd(dx3, x2, g, w_up_t, conv, w_down, sv, tag, layer, riders):
    da = _mm(dx3, w_down, "nt", tag + "_down_dx")
    riders.grad('down%d' % layer, _as_pieces(_mm(sv['a'], dx3, "tn", tag + "_down_dw", out_dtype=BF16)))
    dh, dconv = riders.run(tag + "_act_bwd", _ffn_act_bwd, sv['h'], conv, da)
    dxf = riders.run(tag + "_up_dx", _mm_parts, dh, w_up_t, "nn")
    dw_up_t = riders.run(tag + "_up_dw", _mm_parts, dh, sv['xf'], "tn", out_dtype=BF16)
    riders.grad('up_t%d' % layer, _as_pieces(dw_up_t))
    dx2, dg = riders.run(tag + "_norm_bwd", _rms_bwd, x2, g, dxf, dx3)
    return dx2, dconv, dg


BIG_NAMES, SMALL_NAMES, REP_NAMES = list(BIG_SHARDED), list(SMALL_SHARDED), list(REPLICATED)
SMALL_SIZES = [int(np.prod(_shard_shape(*SMALL_SHARDED[n]))) for n in SMALL_NAMES]


PIECES = [('w_in_t', 384), ('w_glu', 32), ('w_out', 128), ('pool_w', 32), ('wq0', 128), ('wq1', 128),
          ('wkv_t0', 256), ('wkv_t1', 256), ('wo0', 128), ('wo1', 128), ('up_t0', 704), ('up_t1', 704),
          ('down0', 352), ('down1', 352)]
W_IN_ROWS = 4 * WIDTH_A + 2 * N_HEADS_A + SSM_WIDTH
W_IN_PIECE = W_IN_ROWS // N_DEV


def _row_tile(rows):
    return max(t for t in range(16, min(rows, 512) + 1, 16) if rows % t == 0)


class _Riders:
    def __init__(self):
        self.waiting = {}
        self.deferred = {}
        self.grads = {}
        self.groups = []
        self.reduced = {}

    def add(self, host, comm, then):
        self.waiting.setdefault(host, []).append((comm, then))

    def after(self, marker, then):
        self.deferred.setdefault(marker, []).append(then)

    def mark(self, name, out=None):
        for cont in self.deferred.pop(name, []):
            step = cont()
            if step is not None:
                values, then = step
                out, values = lax.optimization_barrier((out, values))
                then(values)
        return out

    def run(self, name, fn, *args, **kw):
        riders = self.waiting.pop(name, [])
        if not riders:
            out = fn(*args, name=name, **kw)
        else:
            out, couts = fn(*args, name=name, comm=[c for c, _ in riders], **kw)
            for (_, then), got in zip(riders, couts):
                then(got)
        return self.mark(name, out)

    def grad(self, key, pieces):
        self.grads[key] = pieces
        for group in [g for g in self.groups if all(k in self.grads for k in g[1])]:
            self.groups.remove(group)
            self._reduce(*group)

    def _reduce(self, name, keys, pair_marker, sum_marker):
        arrays = [self.grads[k] for k in keys]
        rows = sum(a.shape[1] for a in arrays)
        tile = _row_tile(rows)
        from_sibling = _sequencer_swap(arrays, name + "_to_sibling")

        def after_swap():
            if len(arrays) == 1:
                chip_sums = _pair_sum_pieces(arrays[0], from_sibling, name + "_pair_sum", tr=tile)
            else:
                core = lax.axis_index("c")
                keep = jnp.concatenate(
                    [lax.dynamic_index_in_dim(a.reshape(4, 2, a.shape[1], PACK_COLS), core, 1, keepdims=False)
                     for a in arrays], axis=1)
                chip_sums = _pair_sum(keep, from_sibling, name + "_pair_sum", tr=tile)

            def exchange_among_chips(chip_sums):
                from_chips = _sequencer_chips(chip_sums, name + "_to_chips")

                def store(total):
                    off = 0
                    for k, a in zip(keys, arrays):
                        self.reduced[k] = total[off:off + a.shape[1]]
                        off += a.shape[1]

                self.after(sum_marker, lambda: (
                    _chip_sum(chip_sums, from_chips, name + "_chip_sum", tr=tile), store))

            return chip_sums, exchange_among_chips

        self.after(pair_marker, after_swap)


class _Weights:
    def __init__(self, inp):
        bf = lambda a: a.astype(BF16)
        local = {'w_in_t': bf(inp['w_in_ab'][0]).T, 'w_glu': bf(inp['w_glu_b'][0]), 'w_out': bf(inp['w_out_ab'][0]),
                 'pool_w': bf(inp['pool_w'][0]),
                 'small': _pack([inp[n] for n in SMALL_NAMES])}
        for l in range(2):
            local['wq%d' % l] = bf(inp['xa_wq'][l])
            local['wkv_t%d' % l] = bf(inp['xa_wkv'][l]).T
            local['wo%d' % l] = bf(inp['xa_wo'][l])
            local['up_t%d' % l] = bf(inp['ffn_w_up'][l]).T
            local['down%d' % l] = bf(inp['ffn_w_down'][l])
        self.local, self.full = local, {}

    def plan(self, keys):
        return _gather_comm([self.local[k] for k in keys])

    def gather_by_sequencer(self, keys, after, name, collective_id):
        arrays = [self.local[k] for k in keys]
        tie = (after.reshape(-1)[0] * 0.0).astype(arrays[0].dtype)
        arrays[0] = arrays[0] + tie
        self.land(keys, _sequencer_gather(arrays, name, collective_id))

    def land(self, keys, gathered):
        for k, g in zip(keys, gathered):
            if k == 'small':
                off = 0
                for n, size in zip(SMALL_NAMES, SMALL_SIZES):
                    self.full[n] = _merge_shards(g.reshape(N_DEV, -1)[:, off:off + size], *SMALL_SHARDED[n])
                    off += size
            elif k == 'pool_w':
                self.full[k] = jnp.swapaxes(g, 0, 1).reshape(len(POOL_WINDOWS), POOL_GROUP, POOL_GROUP)
            else:
                self.full[k] = g.reshape(N_DEV * g.shape[1], g.shape[2])


GATHER_FIRST = ['w_in_t', 'small']
GATHER_LAYER0 = ['w_glu', 'w_out', 'wq0', 'wkv_t0', 'wo0', 'down0', 'up_t0']
GATHER_LAYER1 = ['pool_w', 'wq1', 'wkv_t1', 'wo1', 'up_t1', 'down1']
GATHER_LAYER0_ID, GATHER_LAYER1_ID = 7, 8
GRAD_RIDES = [('g_down1', ['down1'], 'l1_ffn_up_dx', 'l1_xa_norm_bwd'),
              ('g_up1', ['up_t1'], 'l1_xa_norm_bwd', 'l0_ffn_up_dx'),
              ('g_xa1', ['wq1', 'wkv_t1', 'wo1', 'pool_w'], 'l0_ffn_up_dx', 'l0_xa_norm_bwd'),
              ('g_down0', ['down0'], 'l0_ffn_up_dx', 'l0_s5_scan_bwd'),
              ('g_l0', ['up_t0', 'wq0', 'wkv_t0', 'wo0'], 'l0_s5_cx_dx', 'l0_in_u_dx'),
              ('g_out', ['w_out', 'w_glu'], 'l0_s5_scan_bwd', 'l0_in_u_dx'),
              ('g_in', ['w_in_t'], 'l0_in_u_dx', 'adamw_pool_w')]


def _local_step(inp):
    f32_of = lambda n: inp[n].astype(F32)
    weights = _Weights(inp)
    riders = _Riders()
    riders.groups = list(GRAD_RIDES)
    full = weights.full
    weights.land(GATHER_FIRST, _comm_only(weights.plan(GATHER_FIRST), "gather_first"))
    weights.gather_by_sequencer(GATHER_LAYER0, full['w_in_t'], "gather_layer0", GATHER_LAYER0_ID)
    w_in_t = full['w_in_t']
    wts0 = dict(w_qkv_t=w_in_t[:3 * WIDTH_A], w_gate_t=w_in_t[3 * WIDTH_A:4 * WIDTH_A],
                w_ba_t=jnp.concatenate([w_in_t[4 * WIDTH_A:4 * WIDTH_A + 8], jnp.zeros((LANE - 8, D_MODEL), BF16)], 0),
                w_u_t=w_in_t[4 * WIDTH_A + 8:])
    lb_disc, disc_vjp = jax.vjp(_s5_discretise, f32_of('ssm_lambda_re')[0], f32_of('ssm_lambda_im')[0],
                                f32_of('ssm_b_re')[0], f32_of('ssm_b_im')[0], f32_of('ssm_log_dt')[0])
    b_in, c_out, a_row = _s5_matrices(*lb_disc, f32_of('ssm_c_re')[0], f32_of('ssm_c_im')[0])
    zeros4 = jnp.zeros((1, 4), F32)
    p0 = dict(conv_qkv=full['conv_qkv_a'][0], onorm_g=f32_of('onorm_g_a'),
              arow=jnp.concatenate([zeros4, f32_of('a_log_a'), jnp.zeros((1, LANE - 8), F32)], 1),
              brow=jnp.concatenate([zeros4, f32_of('dt_bias_a'), jnp.zeros((1, LANE - 8), F32)], 1),
              b_in=b_in.astype(BF16), c_out=c_out.astype(BF16), a_row=a_row,
              d_row=f32_of('ssm_d').reshape(1, SSM_WIDTH), b_glu=f32_of('b_glu_b'))

    x0 = inp['x'][0]
    mem_n = _rms_fwd(inp['mem'][0], inp['norm_mem_g'], BF16, "mem_norm")
    xn0 = _rms_fwd(x0, inp['norm_mix_g'][0], BF16, "l0_mix_norm")
    x1, sv_mix0 = _hybrid_fwd(xn0, x0, wts0, p0, weights, riders)
    x2, sv_xa0 = _xa_fwd(x1, inp['norm_xa_g'][0], mem_n, full['wq0'], full['wkv_t0'], full['wo0'], "l0_xa", riders)
    x3, sv_ffn0 = _ffn_fwd(x2, inp['norm_ffn_g'][0], full['up_t0'], full['ffn_conv'][0], full['down0'], "l0_ffn", riders)
    xn1 = _rms_fwd(x3, inp['norm_mix_g'][1], F32, "l1_mix_norm")
    x4 = _pool_fwd(xn1, full['pool_w'], full['pool_scale'], x3, "l1_pool")
    x5, sv_xa1 = _xa_fwd(x4, inp['norm_xa_g'][1], mem_n, full['wq1'], full['wkv_t1'], full['wo1'], "l1_xa", riders)
    x6, sv_ffn1 = _ffn_fwd(x5, inp['norm_ffn_g'][1], full['up_t1'], full['ffn_conv'][1], full['down1'], "l1_ffn", riders)
    loss_part, dx6, dg_final = _loss_head(x6, inp['norm_final_g'], inp['loss_target'][0], "loss_head")

    dx5, dconv1, dg_ffn1 = _ffn_bwd(dx6, x5, inp['norm_ffn_g'][1], full['up_t1'], full['ffn_conv'][1], full['down1'],
                                    sv_ffn1, "l1_ffn", 1, riders)
    dx4, dmem1, dg_xa1 = _xa_bwd(dx5, x4, inp['norm_xa_g'][1], mem_n, full['wq1'], full['wkv_t1'], full['wo1'],
                                 sv_xa1, "l1_xa", 1, riders)
    dxn1, dpool_w, dpool_scale = _pool_bwd(xn1, full['pool_w'], full['pool_scale'], dx4, "l1_pool_bwd")
    pool_pieces = jnp.swapaxes(dpool_w.astype(BF16).reshape(len(POOL_WINDOWS), N_DEV, -1, POOL_GROUP), 0, 1)
    riders.grad('pool_w', pool_pieces.reshape(N_DEV, -1, PACK_COLS))
    dx3, dg_mix1 = riders.run("l1_mix_norm_bwd", _rms_bwd, x3, inp['norm_mix_g'][1], dxn1, dx4)
    dx2, dconv0, dg_ffn0 = _ffn_bwd(dx3, x2, inp['norm_ffn_g'][0], full['up_t0'], full['ffn_conv'][0], full['down0'],
                                    sv_ffn0, "l0_ffn", 0, riders)
    dx1, dmem0, dg_xa0 = _xa_bwd(dx2, x1, inp['norm_xa_g'][0], mem_n, full['wq0'], full['wkv_t0'], full['wo0'],
                                 sv_xa0, "l0_xa", 0, riders)
    dxn0, g_mix0 = _hybrid_bwd(dx1, xn0, wts0, p0, sv_mix0, riders)
    grad_x, dg_mix0 = _rms_bwd(x0, inp['norm_mix_g'][0], dxn0, dx1, "l0_mix_norm_bwd")
    _, dg_mem = _rms_bwd(inp['mem'][0], inp['norm_mem_g'], dmem0 + dmem1, None, "mem_norm_bwd")
    assert not riders.groups and not riders.waiting and all(k.startswith("adamw_") for k in riders.deferred), (
        riders.groups, list(riders.waiting), list(riders.deferred))

    db_in, dc_out, da_row, dd = g_mix0['s5']
    dlb_re, dlb_im, dbb_re, dbb_im, dc_re, dc_im = _s5_matrix_grads(db_in, dc_out, da_row)
    dlam_re, dlam_im, dbr, dbi, dlog_dt = disc_vjp((dlb_re, dlb_im, dbb_re, dbb_im))

    rep_grads = {
        'norm_mix_g': jnp.concatenate([dg_mix0, dg_mix1], 0), 'norm_xa_g': jnp.concatenate([dg_xa0, dg_xa1], 0),
        'norm_ffn_g': jnp.concatenate([dg_ffn0, dg_ffn1], 0), 'norm_mem_g': dg_mem.reshape(-1),
        'norm_final_g': dg_final.reshape(-1), 'a_log_a': g_mix0['a_log_a'], 'dt_bias_a': g_mix0['dt_bias_a'],
        'onorm_g_a': g_mix0['onorm_g_a'], 'ssm_lambda_re': dlam_re[None], 'ssm_lambda_im': dlam_im[None],
        'ssm_b_re': dbr[None], 'ssm_b_im': dbi[None], 'ssm_c_re': dc_re[None], 'ssm_c_im': dc_im[None],
        'ssm_d': dd.reshape(1, N_GROUPS, SSM_GROUP), 'ssm_log_dt': dlog_dt[None], 'b_glu_b': g_mix0['b_glu_b']}
    small_grads = {'conv_qkv_a': g_mix0['conv_qkv_a'][None], 'pool_scale': dpool_scale,
                   'ffn_conv': jnp.stack([dconv0, dconv1])}
    return loss_part, grad_x, riders, rep_grads, small_grads


ADAMW_ORDER = ['ffn_w_up', 'ffn_w_down', 'xa_wkv', 'xa_wq', 'xa_wo', 'w_out_ab', 'w_glu_b', 'pool_w', 'w_in_ab']


def _update(inp, loss_part, grad_x, riders, rep_grads, small_grads):
    dev = _device_index()
    misc_local = _pack([rep_grads[n] for n in REP_NAMES] + [small_grads[n] for n in SMALL_NAMES] + [loss_part])
    (misc_all,) = _sequencer_gather([misc_local], "gather_small_grads", GATHER_LAYER0_ID)
    piece = lambda key: riders.reduced[key]
    both = lambda name: jnp.stack([piece(name + '0'), piece(name + '1')])
    swap = lambda a: jnp.swapaxes(a, -1, -2)
    reduced = {'w_in_ab': lambda: piece('w_in_t')[:W_IN_PIECE][None],
               'w_glu_b': lambda: piece('w_glu').reshape(inp['w_glu_b'].shape),
               'w_out_ab': lambda: piece('w_out')[None], 'pool_w': lambda: piece('pool_w').reshape(inp['pool_w'].shape),
               'xa_wq': lambda: both('wq'), 'xa_wkv': lambda: both('wkv_t'), 'xa_wo': lambda: both('wo'),
               'ffn_w_up': lambda: both('up_t'), 'ffn_w_down': lambda: both('down')}
    transposed = ('w_in_ab', 'xa_wkv', 'ffn_w_up')
    grads, upd = {}, {}
    assert sorted(ADAMW_ORDER) == sorted(BIG_NAMES)
    for n in ADAMW_ORDER:
        fix = swap if n in transposed else (lambda a: a)
        g = reduced[n]()
        out = riders.run("adamw_" + n, _adamw, fix(inp[n]), g, fix(inp['m_' + n]), fix(inp['v_' + n]))
        upd[n], grads[n] = tuple(fix(o) for o in out), fix(g)
    assert not riders.waiting and not riders.deferred, (list(riders.waiting), list(riders.deferred))
    misc_sum = _sum_leading(misc_all, "small_grads_sum")
    misc = _unpack(misc_sum, [inp[n].shape for n in REP_NAMES] + [SMALL_SHARDED[n][0] for n in SMALL_NAMES] + [()])
    loss = misc.pop()
    for n, g in zip(REP_NAMES, misc):
        grads[n] = g
    for n, g in zip(SMALL_NAMES, misc[len(REP_NAMES):]):
        grads[n] = lax.dynamic_index_in_dim(_split_shards(g, SMALL_SHARDED[n][1]), dev, 0, keepdims=False
                                            ).reshape(inp[n].shape)
    tiny_names = REP_NAMES + SMALL_NAMES
    rep_total = sum(int(np.prod(inp[n].shape)) for n in REP_NAMES)
    packs = [_pack([inp[prefix + n] for n in tiny_names]) for prefix in ('', 'm_', 'v_')]
    g_pack = _pack([misc_sum.reshape(-1)[:rep_total]] + [grads[n] for n in SMALL_NAMES])
    tiny_out = [_unpack(o, [inp[n].shape for n in tiny_names])
                for o in _adamw(packs[0], g_pack, packs[1], packs[2], "adamw_small")]
    for i, n in enumerate(tiny_names):
        upd[n] = tuple(o[i] for o in tiny_out)

    outs = [loss, grad_x[None]]
    outs += [grads[n] for n in WEIGHT_NAMES]
    for i in range(3):
        outs += [upd[n][i] for n in WEIGHT_NAMES]
    return tuple(outs)


def _step(inp):
    loss_part, grad_x, riders, rep_grads, small_grads = _local_step(inp)
    return _update(inp, loss_part, grad_x, riders, rep_grads, small_grads)


INPUT_NAMES = (['x', 'mem'] + WEIGHT_NAMES + ['loss_target'] + ['m_' + n for n in WEIGHT_NAMES]
               + ['v_' + n for n in WEIGHT_NAMES])


def kernel(x, mem, norm_mix_g, norm_xa_g, norm_ffn_g, norm_mem_g, norm_final_g, w_in_ab, conv_qkv_a, a_log_a, dt_bias_a, onorm_g_a, ssm_lambda_re, ssm_lambda_im, ssm_b_re, ssm_b_im, ssm_c_re, ssm_c_im, ssm_d, ssm_log_dt, w_glu_b, b_glu_b, w_out_ab, pool_w, pool_scale, xa_wq, xa_wkv, xa_wo, ffn_w_up, ffn_conv, ffn_w_down, loss_target, m_norm_mix_g, m_norm_xa_g, m_norm_ffn_g, m_norm_mem_g, m_norm_final_g, m_w_in_ab, m_conv_qkv_a, m_a_log_a, m_dt_bias_a, m_onorm_g_a, m_ssm_lambda_re, m_ssm_lambda_im, m_ssm_b_re, m_ssm_b_im, m_ssm_c_re, m_ssm_c_im, m_ssm_d, m_ssm_log_dt, m_w_glu_b, m_b_glu_b, m_w_out_ab, m_pool_w, m_pool_scale, m_xa_wq, m_xa_wkv, m_xa_wo, m_ffn_w_up, m_ffn_conv, m_ffn_w_down, v_norm_mix_g, v_norm_xa_g, v_norm_ffn_g, v_norm_mem_g, v_norm_final_g, v_w_in_ab, v_conv_qkv_a, v_a_log_a, v_dt_bias_a, v_onorm_g_a, v_ssm_lambda_re, v_ssm_lambda_im, v_ssm_b_re, v_ssm_b_im, v_ssm_c_re, v_ssm_c_im, v_ssm_d, v_ssm_log_dt, v_w_glu_b, v_b_glu_b, v_w_out_ab, v_pool_w, v_pool_scale, v_xa_wq, v_xa_wkv, v_xa_wo, v_ffn_w_up, v_ffn_conv, v_ffn_w_down):
    args = (x, mem, norm_mix_g, norm_xa_g, norm_ffn_g, norm_mem_g, norm_final_g, w_in_ab, conv_qkv_a, a_log_a, dt_bias_a, onorm_g_a, ssm_lambda_re, ssm_lambda_im, ssm_b_re, ssm_b_im, ssm_c_re, ssm_c_im, ssm_d, ssm_log_dt, w_glu_b, b_glu_b, w_out_ab, pool_w, pool_scale, xa_wq, xa_wkv, xa_wo, ffn_w_up, ffn_conv, ffn_w_down, loss_target, m_norm_mix_g, m_norm_xa_g, m_norm_ffn_g, m_norm_mem_g, m_norm_final_g, m_w_in_ab, m_conv_qkv_a, m_a_log_a, m_dt_bias_a, m_onorm_g_a, m_ssm_lambda_re, m_ssm_lambda_im, m_ssm_b_re, m_ssm_b_im, m_ssm_c_re, m_ssm_c_im, m_ssm_d, m_ssm_log_dt, m_w_glu_b, m_b_glu_b, m_w_out_ab, m_pool_w, m_pool_scale, m_xa_wq, m_xa_wkv, m_xa_wo, m_ffn_w_up, m_ffn_conv, m_ffn_w_down, v_norm_mix_g, v_norm_xa_g, v_norm_ffn_g, v_norm_mem_g, v_norm_final_g, v_w_in_ab, v_conv_qkv_a, v_a_log_a, v_dt_bias_a, v_onorm_g_a, v_ssm_lambda_re, v_ssm_lambda_im, v_ssm_b_re, v_ssm_b_im, v_ssm_c_re, v_ssm_c_im, v_ssm_d, v_ssm_log_dt, v_w_glu_b, v_b_glu_b, v_w_out_ab, v_pool_w, v_pool_scale, v_xa_wq, v_xa_wkv, v_xa_wo, v_ffn_w_up, v_ffn_conv, v_ffn_w_down)
    return _step(dict(zip(INPUT_NAMES, args)))
```

```python
import functools
import math

import numpy as np
import jax
import jax.numpy as jnp
from jax import lax
from jax.experimental import pallas as pl
from jax.experimental.pallas import tpu as pltpu
from jax.experimental.pallas import tpu_sc as plsc

F32, BF16 = jnp.float32, jnp.bfloat16
HIGH, HIGHEST = lax.Precision.HIGH, lax.Precision.HIGHEST
MESH = pl.DeviceIdType.MESH

N_DEV = 8
SEQ, D_MODEL, MEM_LEN = 2048, 1024, 256
WIDTH_A, N_HEADS_A, HEAD_A, CONV_A = 512, 4, 128, 4
GDR_CHUNK = 128
SSM_WIDTH, SSM_GROUP, N_GROUPS, SSM_STATE = 512, 16, 32, 64
SSM_CH = N_GROUPS * SSM_STATE
SCAN_CB = 512
POOL_WINDOWS = (2, 4, 8, 16)
POOL_GROUP = 256
N_HEADS_X, HEAD_X = 4, 256
D_FF, CONV_FFN = 2816, 3
RMS_EPS = 1e-6
ADAM_LR, ADAM_B1, ADAM_B2, ADAM_EPS, ADAM_WD, ADAM_STEP = 0.001, 0.9, 0.999, 1e-08, 0.01, 10
LANE = 128
PACK_COLS = 1024
VMEM_LIMIT_BYTES = 56 * 1024 * 1024


def _params(sem=None):
    return pltpu.CompilerParams(dimension_semantics=sem, vmem_limit_bytes=VMEM_LIMIT_BYTES)


class Comm:
    def __init__(self, inputs, out_shapes, sems, start, end, mid=None):
        self.inputs, self.out_shapes, self.sems = list(inputs), list(out_shapes), list(sems)
        self.start, self.mid, self.end = start, mid, end


def _merge_comms(comms):
    comms = [c for c in comms if c is not None]
    if not comms:
        return None, []
    bounds, ni, no, ns = [], 0, 0, 0
    for c in comms:
        bounds.append((ni, no, ns))
        ni, no, ns = ni + len(c.inputs), no + len(c.out_shapes), ns + len(c.sems)

    def phase(which):
        def run(ins, outs, sems):
            for c, (i0, o0, s0) in zip(comms, bounds):
                fn = getattr(c, which)
                if fn is not None:
                    fn(ins[i0:i0 + len(c.inputs)], outs[o0:o0 + len(c.out_shapes)], sems[s0:s0 + len(c.sems)])
        return run

    merged = Comm([a for c in comms for a in c.inputs], [s for c in comms for s in c.out_shapes],
                  [s for c in comms for s in c.sems], phase("start"), phase("end"), phase("mid"))
    return merged, [(o0, o0 + len(c.out_shapes)) for c, (_, o0, _) in zip(comms, bounds)]


def _call(body, *, name, grid, in_specs, out_specs, out_shape, args, scratch_shapes=(), sem=None, comm=None):
    single = not isinstance(out_shape, (list, tuple))
    out_specs_l = [out_specs] if single else list(out_specs)
    out_shape_l = [out_shape] if single else list(out_shape)
    scratch_shapes = list(scratch_shapes)
    merged, spans = _merge_comms(comm if isinstance(comm, (list, tuple)) else [comm])
    if merged is None:
        outs = pl.pallas_call(body, name=name, grid=grid, in_specs=list(in_specs), out_specs=out_specs_l,
                              out_shape=out_shape_l, scratch_shapes=scratch_shapes, compiler_params=_params(sem))(*args)
        outs = outs[0] if single else outs
        return outs if comm is None else (outs, [])
    n_in, n_out, n_scr = len(in_specs), len(out_specs_l), len(scratch_shapes)
    ci, co = len(merged.inputs), len(merged.out_shapes)
    total = int(np.prod(grid))

    def wrapped(*refs):
        ins, cins = refs[:n_in], refs[n_in:n_in + ci]
        outs, couts = refs[n_in + ci:n_in + ci + n_out], refs[n_in + ci + n_out:n_in + ci + n_out + co]
        scr, csems = refs[n_in + ci + n_out + co:n_in + ci + n_out + co + n_scr], refs[n_in + ci + n_out + co + n_scr:]
        lin = pl.program_id(0)
        for d in range(1, len(grid)):
            lin = lin * grid[d] + pl.program_id(d)
        pl.when(lin == 0)(lambda: merged.start(cins, couts, csems))
        body(*ins, *outs, *scr)
        mid_step = min((3 * total) // 4, total - 1)
        pl.when(lin == mid_step)(lambda: merged.mid(cins, couts, csems))
        pl.when(lin == total - 1)(lambda: merged.end(cins, couts, csems))

    any_spec = pl.BlockSpec(memory_space=pl.ANY)
    res = pl.pallas_call(
        wrapped, name=name, grid=grid, in_specs=list(in_specs) + [any_spec] * ci,
        out_specs=out_specs_l + [any_spec] * co, out_shape=out_shape_l + merged.out_shapes,
        scratch_shapes=scratch_shapes + merged.sems,
        compiler_params=_params(("arbitrary",) * len(grid)))(*args, *merged.inputs)
    outs, couts = res[:n_out], res[n_out:]
    return (outs[0] if single else list(outs)), [list(couts[a:b]) for a, b in spans]


def _comm_only(comm, name):
    def body():
        pass

    _, couts = _call(body, name=name, grid=(1,), in_specs=[], out_specs=[], out_shape=[], args=[], comm=comm)
    return couts[0]


def _tile(dim, pref):
    best = None
    for t in range(LANE, min(dim, pref) + 1, LANE):
        if dim % t == 0:
            best = t
    return best if best is not None else dim


MM_VMEM_BUDGET = 40 * 1024 * 1024


def _mm_tiles(m, n, k, a_bytes, b_bytes, o_bytes, r_bytes):
    for tk in (k, _tile(k, 2048), _tile(k, 1024), _tile(k, 512)):
        for tm, tn in ((1024, 1536), (1024, 1024), (1024, 512), (512, 512), (256, 512), (256, 256)):
            tm, tn = _tile(m, tm), _tile(n, tn)
            acc = 0 if tk == k else tm * tn * 4
            need = 2 * (tm * tk * a_bytes + tk * tn * b_bytes + tm * tn * (o_bytes + r_bytes)) + acc
            if need <= MM_VMEM_BUDGET:
                return tm, tn, tk
    raise ValueError("no matmul tiling fits VMEM")


def _mm(a, b, mode, name, out_dtype=F32, res=None, comm=None):
    if mode == "nn":
        (m, k), n = a.shape, b.shape[1]
    elif mode == "nt":
        (m, k), n = a.shape, b.shape[0]
    else:
        (k, m), n = a.shape, b.shape[1]
    tm, tn, tk = _mm_tiles(m, n, k, a.dtype.itemsize, b.dtype.itemsize, jnp.dtype(out_dtype).itemsize,
                           0 if res is None else res.dtype.itemsize)
    nk = k // tk
    dims = {"nn": ((1,), (0,)), "nt": ((1,), (1,)), "tn": ((0,), (0,))}[mode]

    def body(*refs):
        if res is None:
            a_ref, b_ref, o_ref = refs[:3]
            r_ref = None
        else:
            a_ref, b_ref, r_ref, o_ref = refs[:4]
        part = lax.dot_general(a_ref[...].astype(BF16), b_ref[...].astype(BF16), (dims, ((), ())),
                               preferred_element_type=F32)

        def finish(out):
            if r_ref is not None:
                out = out + r_ref[...].astype(F32)
            o_ref[...] = out.astype(out_dtype)

        if nk == 1:
            finish(part)
            return
        acc = refs[-1]
        kk = pl.program_id(2)

        @pl.when(kk == 0)
        def _():
            acc[...] = part

        @pl.when(kk > 0)
        def _():
            acc[...] += part

        @pl.when(kk == nk - 1)
        def _():
            finish(acc[...])

    a_spec = (pl.BlockSpec((tk, tm), lambda i, j, q: (q, i)) if mode == "tn"
              else pl.BlockSpec((tm, tk), lambda i, j, q: (i, q)))
    b_spec = (pl.BlockSpec((tn, tk), lambda i, j, q: (j, q)) if mode == "nt"
              else pl.BlockSpec((tk, tn), lambda i, j, q: (q, j)))
    o_spec = pl.BlockSpec((tm, tn), lambda i, j, q: (i, j))
    in_specs, args = [a_spec, b_spec], [a, b]
    if res is not None:
        in_specs.append(o_spec)
        args.append(res)
    return _call(body, name=name, grid=(m // tm, n // tn, nk), in_specs=in_specs, out_specs=o_spec,
                 out_shape=jax.ShapeDtypeStruct((m, n), out_dtype),
                 scratch_shapes=[] if nk == 1 else [pltpu.VMEM((tm, tn), F32)],
                 sem=("parallel", "parallel", "arbitrary"), args=args, comm=comm)


def _mm_parts(parts, b, mode, name, out_dtype=F32):
    count = len(parts)
    rows, cols = parts[0].shape
    n = b.shape[1]
    if mode == "nn":
        tm, tn, tk = _mm_tiles(rows, n, count * cols, parts[0].dtype.itemsize, b.dtype.itemsize,
                               jnp.dtype(out_dtype).itemsize, 0)
        assert tk == count * cols

        def body(*refs):
            a_refs, b_refs, o_ref = refs[:count], refs[count:2 * count], refs[2 * count]
            out = None
            for a_ref, b_ref in zip(a_refs, b_refs):
                part = jnp.dot(a_ref[...].astype(BF16), b_ref[...].astype(BF16), preferred_element_type=F32)
                out = part if out is None else out + part
            o_ref[...] = out.astype(out_dtype)

        return _call(body, name=name, grid=(rows // tm, n // tn),
                     in_specs=[pl.BlockSpec((tm, cols), lambda i, j: (i, 0))] * count
                     + [pl.BlockSpec((cols, tn), functools.partial(lambda q, i, j: (q, j), q)) for q in range(count)],
                     out_specs=pl.BlockSpec((tm, tn), lambda i, j: (i, j)),
                     out_shape=jax.ShapeDtypeStruct((rows, n), out_dtype),
                     sem=("parallel", "parallel"), args=(*parts, *([b] * count)))
    tm, tn = _tile(cols, 1536), _tile(n, 512)
    per = cols // tm

    def body_t(*refs):
        a_refs, b_ref, o_ref = refs[:count], refs[count], refs[count + 1]
        for q, a_ref in enumerate(a_refs):
            @pl.when(pl.program_id(0) // per == q)
            def _(a_ref=a_ref):
                o_ref[...] = lax.dot_general(a_ref[...].astype(BF16), b_ref[...].astype(BF16),
                                             (((0,), (0,)), ((), ())), preferred_element_type=F32).astype(out_dtype)

    return _call(body_t, name=name, grid=(count * per, n // tn),
                 in_specs=[pl.BlockSpec((rows, tm), functools.partial(lambda q, i, j: (0, jnp.clip(i - q * per, 0, per - 1)), q))
                           for q in range(count)] + [pl.BlockSpec((rows, tn), lambda i, j: (0, j))],
                 out_specs=pl.BlockSpec((tm, tn), lambda i, j: (i, j)),
                 out_shape=jax.ShapeDtypeStruct((count * cols, n), out_dtype),
                 sem=("parallel", "parallel"), args=(*parts, b))


def _mm_bd(a, b, mode, name, out_dtype=F32, res=None, comm=None, tm=1024):
    if mode == "tn":
        k = a.shape[0]
        nb = min(a.shape[1], b.shape[1]) // LANE
        ma, n = a.shape[1] // nb, b.shape[1] // nb

        def body(a_ref, b_ref, o_ref):
            o_ref[0] = lax.dot_general(a_ref[...].astype(BF16), b_ref[...].astype(BF16), (((0,), (0,)), ((), ())),
                                       preferred_element_type=F32).astype(out_dtype)

        return _call(body, name=name, grid=(nb,),
                     in_specs=[pl.BlockSpec((k, ma), lambda j: (0, j)), pl.BlockSpec((k, n), lambda j: (0, j))],
                     out_specs=pl.BlockSpec((1, ma, n), lambda j: (j, 0, 0)),
                     out_shape=jax.ShapeDtypeStruct((nb, ma, n), out_dtype), sem=("parallel",), args=(a, b), comm=comm)
    m = a.shape[0]
    nb = b.shape[0]
    ka = a.shape[1] // nb
    n = b.shape[2] if mode == "nn" else b.shape[1]
    tm = _tile(m, tm)
    dims = ((1,), (0,)) if mode == "nn" else ((1,), (1,))

    def body(*refs):
        if res is None:
            a_ref, b_ref, o_ref = refs
            r_ref = None
        else:
            a_ref, b_ref, r_ref, o_ref = refs
        out = lax.dot_general(a_ref[...].astype(BF16), b_ref[0].astype(BF16), (dims, ((), ())),
                              preferred_element_type=F32)
        if r_ref is not None:
            out = out + r_ref[...].astype(F32)
        o_ref[...] = out.astype(out_dtype)

    o_spec = pl.BlockSpec((tm, n), lambda i, j: (i, j))
    in_specs = [pl.BlockSpec((tm, ka), lambda i, j: (i, j)), pl.BlockSpec((1,) + b.shape[1:], lambda i, j: (j, 0, 0))]
    args = [a, b]
    if res is not None:
        in_specs.append(o_spec)
        args.append(res)
    return _call(body, name=name, grid=(m // tm, nb), in_specs=in_specs, out_specs=o_spec,
                 out_shape=jax.ShapeDtypeStruct((m, nb * n), out_dtype), sem=("parallel", "parallel"),
                 args=args, comm=comm)


def _rms_fwd(x, g, out_dtype, name, tr=256):
    rows, d = x.shape

    def body(x_ref, g_ref, o_ref):
        xv = x_ref[...]
        r = lax.rsqrt(jnp.mean(xv * xv, axis=-1, keepdims=True) + RMS_EPS)
        o_ref[...] = (xv * r * g_ref[...]).astype(out_dtype)

    return pl.pallas_call(
        body, name=name, grid=(rows // tr,),
        in_specs=[pl.BlockSpec((tr, d), lambda i: (i, 0)), pl.BlockSpec((1, d), lambda i: (0, 0))],
        out_specs=pl.BlockSpec((tr, d), lambda i: (i, 0)), out_shape=jax.ShapeDtypeStruct((rows, d), out_dtype),
        compiler_params=_params(("parallel",)))(x, g.reshape(1, d))


def _rms_bwd(x, g, dy, dres, name, tr=256, comm=None):
    rows, d = x.shape

    def body(*refs):
        if dres is None:
            x_ref, g_ref, dy_ref, dx_ref, dg_ref = refs
            r_ref = None
        else:
            x_ref, g_ref, dy_ref, r_ref, dx_ref, dg_ref = refs

        @pl.when(pl.program_id(0) == 0)
        def _():
            dg_ref[...] = jnp.zeros_like(dg_ref)

        xv, dyv = x_ref[...], dy_ref[...].astype(F32)
        r = lax.rsqrt(jnp.mean(xv * xv, axis=-1, keepdims=True) + RMS_EPS)
        xh = xv * r
        dyg = dyv * g_ref[...]
        dx = r * (dyg - xh * jnp.mean(dyg * xh, axis=-1, keepdims=True))
        if r_ref is not None:
            dx = dx + r_ref[...]
        dx_ref[...] = dx
        dg_ref[...] += jnp.sum(dyv * xh, axis=0, keepdims=True)

    blk = pl.BlockSpec((tr, d), lambda i: (i, 0))
    vec = pl.BlockSpec((1, d), lambda i: (0, 0))
    in_specs, args = [blk, vec, blk], [x, g.reshape(1, d), dy]
    if dres is not None:
        in_specs.append(blk)
        args.append(dres)
    return _call(
        body, name=name, grid=(rows // tr,), in_specs=in_specs, out_specs=[blk, vec],
        out_shape=[jax.ShapeDtypeStruct((rows, d), F32), jax.ShapeDtypeStruct((1, d), F32)],
        sem=("arbitrary",), args=args, comm=comm)


def _loss_head(x, g, target, name, tr=256):
    rows, d = x.shape

    def body(x_ref, g_ref, t_ref, loss_ref, dx_ref, dg_ref):
        @pl.when(pl.program_id(0) == 0)
        def _():
            dg_ref[...] = jnp.zeros_like(dg_ref)
            loss_ref[...] = jnp.zeros_like(loss_ref)

        xv = x_ref[...]
        r = lax.rsqrt(jnp.mean(xv * xv, axis=-1, keepdims=True) + RMS_EPS)
        xh = xv * r
        err = xh * g_ref[...] - t_ref[...]
        loss_ref[...] += 0.5 * jnp.sum(jnp.mean(err * err, axis=-1, keepdims=True), keepdims=True)
        dyv = err * (1.0 / d)
        dyg = dyv * g_ref[...]
        dx_ref[...] = r * (dyg - xh * jnp.mean(dyg * xh, axis=-1, keepdims=True))
        dg_ref[...] += jnp.sum(dyv * xh, axis=0, keepdims=True)

    blk = pl.BlockSpec((tr, d), lambda i: (i, 0))
    vec = pl.BlockSpec((1, d), lambda i: (0, 0))
    return pl.pallas_call(
        body, name=name, grid=(rows // tr,), in_specs=[blk, vec, blk],
        out_specs=[pl.BlockSpec((1, 1), lambda i: (0, 0)), blk, vec],
        out_shape=[jax.ShapeDtypeStruct((1, 1), F32), jax.ShapeDtypeStruct((rows, d), F32),
                   jax.ShapeDtypeStruct((1, d), F32)],
        compiler_params=_params(("arbitrary",)))(x, g.reshape(1, d), target)


def _shift_down(x, s):
    rows = lax.broadcasted_iota(jnp.int32, x.shape, 0)
    return jnp.where(rows >= s, pltpu.roll(x, s, 0), 0.0)


def _shift_up(x, s):
    n = x.shape[0]
    rows = lax.broadcasted_iota(jnp.int32, x.shape, 0)
    return jnp.where(rows < n - s, pltpu.roll(x, n - s, 0), 0.0)


def _sigmoid(x):
    return 1.0 / (1.0 + jnp.exp(-x))


def _silu_and_grad(x):
    s = _sigmoid(x)
    return x * s, s * (1.0 + x * (1.0 - s))


_GELU_C0, _GELU_C1 = math.sqrt(2.0 / math.pi), 0.044715


def _gelu_and_grad(x):
    th = jnp.tanh(_GELU_C0 * (x + _GELU_C1 * x * x * x))
    y = 0.5 * x * (1.0 + th)
    dy = 0.5 * (1.0 + th) + 0.5 * x * (1.0 - th * th) * _GELU_C0 * (1.0 + 3.0 * _GELU_C1 * x * x)
    return y, dy


def _ffn_act_fwd(h, w, name, tc=256, comm=None):
    t = h.shape[0]
    nb = D_FF // tc

    def body(hg_ref, hv_ref, wg_ref, wv_ref, a_ref):
        def conv(x, wr):
            return wr[2:3, :] * x + wr[1:2, :] * _shift_down(x, 1) + wr[0:1, :] * _shift_down(x, 2)

        cg = conv(hg_ref[...], wg_ref[...])
        cv = conv(hv_ref[...], wv_ref[...])
        a_ref[...] = (cg * _sigmoid(cg) * cv).astype(BF16)

    return _call(
        body, name=name, grid=(nb,),
        in_specs=[pl.BlockSpec((t, tc), lambda j: (0, j)), pl.BlockSpec((t, tc), lambda j: (0, j + nb)),
                  pl.BlockSpec((CONV_FFN, tc), lambda j: (0, j)), pl.BlockSpec((CONV_FFN, tc), lambda j: (0, j + nb))],
        out_specs=pl.BlockSpec((t, tc), lambda j: (0, j)), out_shape=jax.ShapeDtypeStruct((t, D_FF), BF16),
        sem=("parallel",), args=(h, h, w, w), comm=comm)


def _ffn_act_bwd(h, w, da, name, tc=256, comm=None):
    t = h.shape[0]
    nb = D_FF // tc

    def body(hg_ref, hv_ref, wg_ref, wv_ref, da_ref, dhg_ref, dhv_ref, dwg_ref, dwv_ref):
        hg, hv, wg, wv = hg_ref[...], hv_ref[...], wg_ref[...], wv_ref[...]
        hg1, hg2, hv1, hv2 = _shift_down(hg, 1), _shift_down(hg, 2), _shift_down(hv, 1), _shift_down(hv, 2)
        cg = wg[2:3, :] * hg + wg[1:2, :] * hg1 + wg[0:1, :] * hg2
        cv = wv[2:3, :] * hv + wv[1:2, :] * hv1 + wv[0:1, :] * hv2
        sg, dsg = _silu_and_grad(cg)
        dav = da_ref[...].astype(F32)
        dcv = dav * sg
        dcg = dav * cv * dsg

        def conv_t(dc, wr):
            return wr[2:3, :] * dc + wr[1:2, :] * _shift_up(dc, 1) + wr[0:1, :] * _shift_up(dc, 2)

        dhg_ref[...] = conv_t(dcg, wg).astype(BF16)
        dhv_ref[...] = conv_t(dcv, wv).astype(BF16)
        dwg_ref[0:1, :] = jnp.sum(dcg * hg2, axis=0, keepdims=True)
        dwg_ref[1:2, :] = jnp.sum(dcg * hg1, axis=0, keepdims=True)
        dwg_ref[2:3, :] = jnp.sum(dcg * hg, axis=0, keepdims=True)
        dwv_ref[0:1, :] = jnp.sum(dcv * hv2, axis=0, keepdims=True)
        dwv_ref[1:2, :] = jnp.sum(dcv * hv1, axis=0, keepdims=True)
        dwv_ref[2:3, :] = jnp.sum(dcv * hv, axis=0, keepdims=True)

    big = lambda off: pl.BlockSpec((t, tc), lambda j: (0, j + off))
    small = lambda off: pl.BlockSpec((CONV_FFN, tc), lambda j: (0, j + off))
    res = _call(
        body, name=name, grid=(nb,),
        in_specs=[big(0), big(nb), small(0), small(nb), big(0)],
        out_specs=[big(0), big(0), small(0), small(0)],
        out_shape=[jax.ShapeDtypeStruct((t, D_FF), BF16), jax.ShapeDtypeStruct((t, D_FF), BF16),
                   jax.ShapeDtypeStruct((CONV_FFN, D_FF), F32), jax.ShapeDtypeStruct((CONV_FFN, D_FF), F32)],
        sem=("parallel",), args=(h, h, w, w, da), comm=comm)
    (dhg, dhv, dwg, dwv), couts = res if comm is not None else (res, None)
    out = ((dhg, dhv), jnp.concatenate([dwg, dwv], axis=1))
    return out if comm is None else (out, couts)


def _attn_probs(q, k):
    s = lax.dot_general(q.astype(BF16), k.astype(BF16), (((1,), (1,)), ((), ())),
                        preferred_element_type=F32) * (HEAD_X ** -0.5)
    s = s - jnp.max(s, axis=-1, keepdims=True)
    p = jnp.exp(s)
    return p / jnp.sum(p, axis=-1, keepdims=True)


def _attn_fwd(q, kv, name, tq=512, comm=None):
    t = q.shape[0]

    def body(q_ref, k_ref, v_ref, o_ref):
        p = _attn_probs(q_ref[...], k_ref[...])
        o_ref[...] = jnp.dot(p.astype(BF16), v_ref[...].astype(BF16), preferred_element_type=F32).astype(BF16)

    return _call(
        body, name=name, grid=(N_HEADS_X, t // tq),
        in_specs=[pl.BlockSpec((tq, HEAD_X), lambda h, i: (i, h)),
                  pl.BlockSpec((MEM_LEN, HEAD_X), lambda h, i: (0, h)),
                  pl.BlockSpec((MEM_LEN, HEAD_X), lambda h, i: (0, h + N_HEADS_X))],
        out_specs=pl.BlockSpec((tq, HEAD_X), lambda h, i: (i, h)),
        out_shape=jax.ShapeDtypeStruct((t, N_HEADS_X * HEAD_X), BF16),
        sem=("parallel", "parallel"), args=(q, kv, kv), comm=comm)


def _attn_bwd(q, kv, do, name, tq=512):
    t = q.shape[0]

    def body(q_ref, k_ref, v_ref, do_ref, dq_ref, dk_ref, dv_ref):
        @pl.when(pl.program_id(1) == 0)
        def _():
            dk_ref[...] = jnp.zeros_like(dk_ref)
            dv_ref[...] = jnp.zeros_like(dv_ref)

        qb, kb, vb, dob = (r[...].astype(BF16) for r in (q_ref, k_ref, v_ref, do_ref))
        p = _attn_probs(qb, kb)
        dp = lax.dot_general(dob, vb, (((1,), (1,)), ((), ())), preferred_element_type=F32)
        ds = p * (dp - jnp.sum(dp * p, axis=-1, keepdims=True)) * (HEAD_X ** -0.5)
        dsb = ds.astype(BF16)
        dq_ref[...] = jnp.dot(dsb, kb, preferred_element_type=F32).astype(BF16)
        dk_ref[...] += lax.dot_general(dsb, qb, (((0,), (0,)), ((), ())), preferred_element_type=F32)
        dv_ref[...] += lax.dot_general(p.astype(BF16), dob, (((0,), (0,)), ((), ())), preferred_element_type=F32)

    qs = pl.BlockSpec((tq, HEAD_X), lambda h, i: (i, h))
    ms = pl.BlockSpec((MEM_LEN, HEAD_X), lambda h, i: (0, h))
    return pl.pallas_call(
        body, name=name, grid=(N_HEADS_X, t // tq),
        in_specs=[qs, ms, pl.BlockSpec((MEM_LEN, HEAD_X), lambda h, i: (0, h + N_HEADS_X)), qs],
        out_specs=[qs, ms, ms],
        out_shape=[jax.ShapeDtypeStruct((t, D_MODEL), BF16), jax.ShapeDtypeStruct((MEM_LEN, D_MODEL), F32),
                   jax.ShapeDtypeStruct((MEM_LEN, D_MODEL), F32)],
        compiler_params=_params(("parallel", "arbitrary")))(q, kv, kv, do)


def _pool_counts(t, win):
    pos = lax.broadcasted_iota(jnp.int32, (t, 1), 0).astype(F32) + 1.0
    return 1.0 / jnp.minimum(pos, float(win))


def _pool_delta(xv, win):
    s, step = xv, 1
    while step < win:
        s = s + _shift_down(s, step)
        step *= 2
    return s * _pool_counts(xv.shape[0], win) - xv


def _pool_delta_t(dv, win):
    s, step = dv * _pool_counts(dv.shape[0], win), 1
    while step < win:
        s = s + _shift_up(s, step)
        step *= 2
    return s - dv


def _pool_fwd(xn, w, scale, res, name):
    t = xn.shape[0]

    def make_branch(win, xn_ref, w_ref, s_ref, r_ref, o_ref):
        def branch():
            dl = _pool_delta(xn_ref[...], win)
            y = jnp.dot(dl.astype(BF16), w_ref[0], preferred_element_type=F32)
            o_ref[...] = r_ref[...] + y * s_ref[...]
        return branch

    def body(xn_ref, w_ref, s_ref, r_ref, o_ref):
        for gi, win in enumerate(POOL_WINDOWS):
            pl.when(pl.program_id(0) == gi)(make_branch(win, xn_ref, w_ref, s_ref, r_ref, o_ref))

    blk = pl.BlockSpec((t, POOL_GROUP), lambda g: (0, g))
    return pl.pallas_call(
        body, name=name, grid=(len(POOL_WINDOWS),),
        in_specs=[blk, pl.BlockSpec((1, POOL_GROUP, POOL_GROUP), lambda g: (g, 0, 0)),
                  pl.BlockSpec((1, POOL_GROUP), lambda g: (0, g)), blk],
        out_specs=blk, out_shape=jax.ShapeDtypeStruct((t, D_MODEL), F32),
        compiler_params=_params(("parallel",)))(xn, w, scale, res)


def _pool_bwd(xn, w, scale, dmix, name):
    t = xn.shape[0]

    def make_branch(win, xn_ref, w_ref, s_ref, d_ref, dxn_ref, dw_ref, ds_ref):
        def branch():
            dl = _pool_delta(xn_ref[...], win).astype(BF16)
            wv = w_ref[0]
            dm = d_ref[...]
            y = jnp.dot(dl, wv, preferred_element_type=F32)
            ds_ref[...] = jnp.sum(dm * y, axis=0, keepdims=True)
            dy = (dm * s_ref[...]).astype(BF16)
            dw_ref[0] = lax.dot_general(dl, dy, (((0,), (0,)), ((), ())), preferred_element_type=F32)
            ddl = lax.dot_general(dy, wv, (((1,), (1,)), ((), ())), preferred_element_type=F32)
            dxn_ref[...] = _pool_delta_t(ddl, win)
        return branch

    def body(*refs):
        for gi, win in enumerate(POOL_WINDOWS):
            pl.when(pl.program_id(0) == gi)(make_branch(win, *refs))

    blk = pl.BlockSpec((t, POOL_GROUP), lambda g: (0, g))
    wspec = pl.BlockSpec((1, POOL_GROUP, POOL_GROUP), lambda g: (g, 0, 0))
    vec = pl.BlockSpec((1, POOL_GROUP), lambda g: (0, g))
    return pl.pallas_call(
        body, name=name, grid=(len(POOL_WINDOWS),), in_specs=[blk, wspec, vec, blk], out_specs=[blk, wspec, vec],
        out_shape=[jax.ShapeDtypeStruct((t, D_MODEL), F32),
                   jax.ShapeDtypeStruct((len(POOL_WINDOWS), POOL_GROUP, POOL_GROUP), F32),
                   jax.ShapeDtypeStruct((1, D_MODEL), F32)],
        compiler_params=_params(("parallel",)))(xn, w, scale, dmix)


def _qkv_conv(h, wr):
    return (wr[3:4, :] * h + wr[2:3, :] * _shift_down(h, 1) + wr[1:2, :] * _shift_down(h, 2)
            + wr[0:1, :] * _shift_down(h, 3))


def _qkv_block_kind(j):
    return j < 2 * N_HEADS_A, jnp.where(j < N_HEADS_A, HEAD_A ** -0.5, 1.0)


def _qkv_pre_fwd(h, w, name):
    t, cols = h.shape

    def body(h_ref, w_ref, o_ref):
        normalised, scale = _qkv_block_kind(pl.program_id(0))
        c = _qkv_conv(h_ref[...], w_ref[...])
        s = c * _sigmoid(c)
        r = lax.rsqrt(jnp.sum(s * s, axis=-1, keepdims=True) + 1e-6)
        o_ref[...] = jnp.where(normalised, s * (r * scale), s)

    blk = pl.BlockSpec((t, HEAD_A), lambda j: (0, j))
    return pl.pallas_call(
        body, name=name, grid=(cols // HEAD_A,), in_specs=[blk, pl.BlockSpec((CONV_A, HEAD_A), lambda j: (0, j))],
        out_specs=blk, out_shape=jax.ShapeDtypeStruct((t, cols), F32), compiler_params=_params(("parallel",)))(h, w)


def _qkv_pre_bwd(h, w, dy, name):
    t, cols = h.shape

    def body(h_ref, w_ref, dy_ref, dh_ref, dw_ref):
        normalised, scale = _qkv_block_kind(pl.program_id(0))
        hv, wr, dyv = h_ref[...], w_ref[...], dy_ref[...]
        h1, h2, h3 = _shift_down(hv, 1), _shift_down(hv, 2), _shift_down(hv, 3)
        c = wr[3:4, :] * hv + wr[2:3, :] * h1 + wr[1:2, :] * h2 + wr[0:1, :] * h3
        s, dsilu = _silu_and_grad(c)
        r = lax.rsqrt(jnp.sum(s * s, axis=-1, keepdims=True) + 1e-6)
        y = s * r
        dys = dyv * scale
        ds = jnp.where(normalised, r * (dys - y * jnp.sum(dys * y, axis=-1, keepdims=True)), dyv)
        dc = ds * dsilu
        dh = (wr[3:4, :] * dc + wr[2:3, :] * _shift_up(dc, 1) + wr[1:2, :] * _shift_up(dc, 2)
              + wr[0:1, :] * _shift_up(dc, 3))
        dh_ref[...] = dh.astype(BF16)
        dw_ref[0:1, :] = jnp.sum(dc * h3, axis=0, keepdims=True)
        dw_ref[1:2, :] = jnp.sum(dc * h2, axis=0, keepdims=True)
        dw_ref[2:3, :] = jnp.sum(dc * h1, axis=0, keepdims=True)
        dw_ref[3:4, :] = jnp.sum(dc * hv, axis=0, keepdims=True)

    blk = pl.BlockSpec((t, HEAD_A), lambda j: (0, j))
    taps = pl.BlockSpec((CONV_A, HEAD_A), lambda j: (0, j))
    return pl.pallas_call(
        body, name=name, grid=(cols // HEAD_A,), in_specs=[blk, taps, blk], out_specs=[blk, taps],
        out_shape=[jax.ShapeDtypeStruct((t, cols), BF16), jax.ShapeDtypeStruct((CONV_A, cols), F32)],
        compiler_params=_params(("parallel",)))(h, w, dy)


def _softplus(x):
    return jnp.maximum(x, 0.0) + jnp.log1p(jnp.exp(-jnp.abs(x)))


def _gates_fwd(ba, arow, brow, name):
    t = ba.shape[0]

    def body(x_ref, a_ref, b_ref, o_ref):
        xv = x_ref[...]
        lane = lax.broadcasted_iota(jnp.int32, xv.shape, 1)
        beta = _sigmoid(xv)
        g = -jnp.exp(a_ref[...]) * _softplus(xv + b_ref[...])
        o_ref[...] = jnp.where(lane < N_HEADS_A, beta, jnp.where(lane < 2 * N_HEADS_A, g, 0.0))

    return pl.pallas_call(body, name=name, out_shape=jax.ShapeDtypeStruct((t, LANE), F32),
                          compiler_params=_params())(ba, arow, brow)


def _gates_bwd(ba, arow, brow, dgb, name):
    t = ba.shape[0]

    def body(x_ref, a_ref, b_ref, d_ref, dx_ref, da_ref, db_ref):
        xv = x_ref[...]
        dv = d_ref[0] + d_ref[1] + d_ref[2] + d_ref[3]
        lane = lax.broadcasted_iota(jnp.int32, xv.shape, 1)
        beta = _sigmoid(xv)
        ea = jnp.exp(a_ref[...])
        z = xv + b_ref[...]
        dgv = jnp.where((lane >= N_HEADS_A) & (lane < 2 * N_HEADS_A), dv, 0.0) * (-ea)
        dz = dgv * _sigmoid(z)
        dx = jnp.where(lane < N_HEADS_A, dv * beta * (1.0 - beta), dz)
        dx_ref[...] = dx.astype(BF16)
        db_ref[...] = jnp.sum(dz, axis=0, keepdims=True)
        da_ref[...] = jnp.sum(dgv * _softplus(z), axis=0, keepdims=True)

    return pl.pallas_call(
        body, name=name,
        out_shape=[jax.ShapeDtypeStruct((t, LANE), BF16), jax.ShapeDtypeStruct((1, LANE), F32),
                   jax.ShapeDtypeStruct((1, LANE), F32)],
        compiler_params=_params())(ba, arow, brow, dgb)


def _head_gates(gates, head):
    lane = lax.broadcasted_iota(jnp.int32, gates.shape, 1)
    beta = jnp.sum(jnp.where(lane == head, gates, 0.0), axis=1, keepdims=True)
    g = jnp.sum(jnp.where(lane == head + N_HEADS_A, gates, 0.0), axis=1, keepdims=True)
    return beta, g


_B_NN, _B_NT, _B_TN = ((2,), (1,)), ((2,), (2,)), ((1,), (1,))


def _bdot(a, b, dims=_B_NN, prec=None):
    if prec is None:
        a, b = a.astype(BF16), b.astype(BF16)
    return lax.dot_general(a, b, (dims, ((0,), (0,))), precision=prec, preferred_element_type=F32)


def _heads_of(ref):
    return jnp.stack([ref[:, h * HEAD_A:(h + 1) * HEAD_A] for h in range(N_HEADS_A)])


def _all_head_gates(gates):
    pairs = [_head_gates(gates, h) for h in range(N_HEADS_A)]
    return jnp.stack([b for b, _ in pairs]), jnp.stack([g for _, g in pairs])


def _gdr_terms(k, beta, g):
    h, c = k.shape[0], GDR_CHUNK
    row = lax.broadcasted_iota(jnp.int32, (c, c), 0)
    col = lax.broadcasted_iota(jnp.int32, (c, c), 1)
    causal, strict = row >= col, row > col
    lower = jnp.broadcast_to(causal.astype(F32), (h, c, c))
    gcum = _bdot(lower, jnp.broadcast_to(g, (h, c, c)), prec=HIGHEST)
    diff = gcum - jnp.swapaxes(gcum, 1, 2)
    decay = jnp.where(causal, jnp.exp(jnp.where(causal, diff, 0.0)), 0.0)
    kb = k * beta
    return row, col, causal, strict, gcum, decay, kb, _bdot(kb, k, _B_NT)


def _unit_lower_inverses(a):
    c = a.shape[1]
    eye = (lax.broadcasted_iota(jnp.int32, (c, c), 0) == lax.broadcasted_iota(jnp.int32, (c, c), 1)).astype(F32)
    p = -a
    inv = eye + p
    step = 1
    while 2 * step < c:
        p = _bdot(p, p, prec=HIGH)
        inv = inv + _bdot(inv, p, prec=HIGH)
        step *= 2
    return inv


def _gdr_fwd(qkv, gates, name, comm=None):
    t = qkv.shape[0]
    c, nh = GDR_CHUNK, N_HEADS_A
    n = t // c

    def body(q_ref, k_ref, v_ref, gb_ref, o_ref, tm_ref, s_ref, state):
        @pl.when(pl.program_id(0) == 0)
        def _():
            state[...] = jnp.zeros_like(state)

        qv, kv, vv = _heads_of(q_ref), _heads_of(k_ref), _heads_of(v_ref)
        beta, g = _all_head_gates(gb_ref[...])
        row, col, causal, strict, gcum, decay, kb, kk = _gdr_terms(kv, beta, g)
        tm = _unit_lower_inverses(jnp.where(strict, kk * decay, 0.0))
        e = jnp.exp(gcum)
        u = _bdot(tm, vv * beta, prec=HIGH)
        w = _bdot(tm, kb * e, prec=HIGH)
        p = jnp.where(causal, _bdot(qv, kv, _B_NT) * decay, 0.0)
        s = state[...]
        s_ref[:, 0] = s
        tm_ref[:, 0] = tm
        vn = u - _bdot(w, s)
        o = _bdot(qv * e, s) + _bdot(p, vn)
        for h in range(nh):
            o_ref[:, h * HEAD_A:(h + 1) * HEAD_A] = o[h]
        glast = gcum[:, c - 1:c, :]
        state[...] = s * jnp.exp(glast) + _bdot(kv * jnp.exp(glast - gcum), vn, _B_TN)

    part = lambda p: pl.BlockSpec((c, WIDTH_A), lambda i: (i, p))
    mat = pl.BlockSpec((nh, 1, c, c), lambda i: (0, i, 0, 0))
    return _call(
        body, name=name, grid=(n,), in_specs=[part(0), part(1), part(2), pl.BlockSpec((c, LANE), lambda i: (i, 0))],
        out_specs=[part(0), mat, mat],
        out_shape=[jax.ShapeDtypeStruct((t, WIDTH_A), F32), jax.ShapeDtypeStruct((nh, n, c, c), F32),
                   jax.ShapeDtypeStruct((nh, n, HEAD_A, HEAD_A), F32)],
        scratch_shapes=[pltpu.VMEM((nh, HEAD_A, HEAD_A), F32)], sem=("arbitrary",),
        args=(qkv, qkv, qkv, gates), comm=comm)


def _gdr_bwd(qkv, gates, tm_all, s_all, do, name, comm=None):
    t = qkv.shape[0]
    c, nh = GDR_CHUNK, N_HEADS_A
    n = t // c

    def body(q_ref, k_ref, v_ref, gb_ref, tm_ref, s_ref, do_ref, dqkv_ref, dgb_ref, dstate):
        @pl.when(pl.program_id(0) == 0)
        def _():
            dstate[...] = jnp.zeros_like(dstate)

        qv, kv, vv, dov = _heads_of(q_ref), _heads_of(k_ref), _heads_of(v_ref), _heads_of(do_ref)
        beta, g = _all_head_gates(gb_ref[...])
        tm, s, dsp = tm_ref[:, 0], s_ref[:, 0], dstate[...]
        row, col, causal, strict, gcum, decay, kb, kk = _gdr_terms(kv, beta, g)
        rowsum = lambda x: jnp.sum(x, axis=2, keepdims=True)
        e = jnp.exp(gcum)
        vb, kbe = vv * beta, kb * e
        u = _bdot(tm, vb, prec=HIGH)
        w = _bdot(tm, kbe, prec=HIGH)
        qk = _bdot(qv, kv, _B_NT)
        p = jnp.where(causal, qk * decay, 0.0)
        vn = u - _bdot(w, s)
        glast = gcum[:, c - 1:c, :]
        el = jnp.exp(glast)
        f = jnp.exp(glast - gcum)
        kd = kv * f
        qe = qv * e

        dvn = _bdot(p, dov, _B_TN) + _bdot(kd, dsp)
        dglast = el[:, :, 0:1] * jnp.sum(s * dsp, axis=(1, 2), keepdims=True)
        dkd = _bdot(vn, dsp, _B_NT)
        dk = dkd * f
        df = rowsum(dkd * kv) * f[:, :, 0:1]
        dglast = dglast + jnp.sum(df, axis=1, keepdims=True)
        dgc = -df
        dp = jnp.where(causal, _bdot(dov, vn, _B_NT), 0.0)
        dqe = _bdot(dov, s, _B_NT)
        dq = dqe * e
        de = rowsum(dqe * qv)
        dstate[...] = dsp * el + _bdot(qe, dov, _B_TN) - _bdot(w, dvn, _B_TN)
        dw = -_bdot(dvn, s, _B_NT)
        dvb = _bdot(tm, dvn, _B_TN, prec=HIGH)
        dkbe = _bdot(tm, dw, _B_TN, prec=HIGH)
        da = -jnp.where(strict, _bdot(dvb, u, _B_NT) + _bdot(dkbe, w, _B_NT), 0.0)
        dkk = da * decay
        dqk = dp * decay
        dd = da * kk + dp * qk
        dq = dq + _bdot(dqk, kv)
        dk = dk + _bdot(dqk, qv, _B_TN)
        dkb = _bdot(dkk, kv) + dkbe * e
        dk = dk + _bdot(dkk, kb, _B_TN)
        de = de + rowsum(dkbe * kb)
        dk = dk + dkb * beta
        dbeta = rowsum(dkb * kv) + rowsum(dvb * vv)
        m = dd * decay
        dgc = dgc + rowsum(m) - rowsum(jnp.swapaxes(m, 1, 2))
        dgc = dgc + de * e[:, :, 0:1]
        dgc = dgc + jnp.where(row[:, 0:1] == c - 1, dglast, 0.0)
        upper = jnp.broadcast_to((row <= col).astype(F32), (nh, c, c))
        dg = _bdot(upper, jnp.broadcast_to(dgc, (nh, c, c)), prec=HIGHEST)
        dv = dvb * beta
        for p, grad in enumerate((dq, dk, dv)):
            for h in range(nh):
                dqkv_ref[:, p * WIDTH_A + h * HEAD_A:p * WIDTH_A + (h + 1) * HEAD_A] = grad[h]
        head = lax.broadcasted_iota(jnp.int32, (nh, c, LANE), 0)
        lane = lax.broadcasted_iota(jnp.int32, (nh, c, LANE), 2)
        dgb_ref[...] = jnp.where(lane == head, dbeta, jnp.where(lane == head + nh, dg, 0.0))

    part = lambda p: pl.BlockSpec((c, WIDTH_A), lambda i: (n - 1 - i, p))
    mat = pl.BlockSpec((nh, 1, c, c), lambda i: (0, n - 1 - i, 0, 0))
    return _call(
        body, name=name, grid=(n,),
        in_specs=[part(0), part(1), part(2), pl.BlockSpec((c, LANE), lambda i: (n - 1 - i, 0)), mat, mat, part(0)],
        out_specs=[pl.BlockSpec((c, 3 * WIDTH_A), lambda i: (n - 1 - i, 0)),
                   pl.BlockSpec((nh, c, LANE), lambda i: (0, n - 1 - i, 0))],
        out_shape=[jax.ShapeDtypeStruct((t, 3 * WIDTH_A), F32), jax.ShapeDtypeStruct((nh, t, LANE), F32)],
        scratch_shapes=[pltpu.VMEM((nh, HEAD_A, HEAD_A), F32)], sem=("arbitrary",),
        args=(qkv, qkv, qkv, gates, tm_all, s_all, do), comm=comm)


def _onorm_fwd(o, gate, g, name):
    t = o.shape[0]

    def body(o_ref, gate_ref, g_ref, y_ref):
        ov, gv = o_ref[...], gate_ref[...]
        r = lax.rsqrt(jnp.mean(ov * ov, axis=-1, keepdims=True) + RMS_EPS)
        y_ref[...] = (ov * r * g_ref[...] * gv * _sigmoid(gv)).astype(BF16)

    blk = pl.BlockSpec((t, HEAD_A), lambda j: (0, j))
    return pl.pallas_call(
        body, name=name, grid=(N_HEADS_A,), in_specs=[blk, blk, pl.BlockSpec((1, HEAD_A), lambda j: (0, 0))],
        out_specs=blk, out_shape=jax.ShapeDtypeStruct((t, WIDTH_A), BF16),
        compiler_params=_params(("parallel",)))(o, gate, g)


def _onorm_bwd(o, gate, g, dy, name):
    t = o.shape[0]

    def body(o_ref, gate_ref, g_ref, dy_ref, do_ref, dgate_ref, dg_ref):
        @pl.when(pl.program_id(0) == 0)
        def _():
            dg_ref[...] = jnp.zeros_like(dg_ref)

        ov, gv, dyv = o_ref[...], gate_ref[...], dy_ref[...].astype(F32)
        r = lax.rsqrt(jnp.mean(ov * ov, axis=-1, keepdims=True) + RMS_EPS)
        oh = ov * r
        sg, dsg = _silu_and_grad(gv)
        dgate_ref[...] = (dyv * oh * g_ref[...] * dsg).astype(BF16)
        dn = dyv * sg
        dg_ref[...] += jnp.sum(dn * oh, axis=0, keepdims=True)
        dng = dn * g_ref[...]
        do_ref[...] = r * (dng - oh * jnp.mean(dng * oh, axis=-1, keepdims=True))

    blk = pl.BlockSpec((t, HEAD_A), lambda j: (0, j))
    vec = pl.BlockSpec((1, HEAD_A), lambda j: (0, 0))
    return pl.pallas_call(
        body, name=name, grid=(N_HEADS_A,), in_specs=[blk, blk, vec, blk], out_specs=[blk, blk, vec],
        out_shape=[jax.ShapeDtypeStruct((t, WIDTH_A), F32), jax.ShapeDtypeStruct((t, WIDTH_A), BF16),
                   jax.ShapeDtypeStruct((1, HEAD_A), F32)],
        compiler_params=_params(("arbitrary",)))(o, gate, g, dy)


def _cmul(ar, ai, br, bi):
    return ar * br - ai * bi, ar * bi + ai * br


def _scan_tables(ar, ai, reverse):
    p1 = (ar, ai)
    p2 = _cmul(*p1, *p1)
    p4 = _cmul(*p2, *p2)
    p8 = _cmul(*p4, *p4)
    p3 = _cmul(*p2, *p1)
    p5 = _cmul(*p4, *p1)
    p6 = _cmul(*p4, *p2)
    p7 = _cmul(*p4, *p3)
    pows = [p1, p2, p3, p4, p5, p6, p7, p8]
    rows = lax.broadcasted_iota(jnp.int32, (8, ar.shape[1]), 0)
    tr = jnp.zeros((8, ar.shape[1]), F32)
    ti = jnp.zeros((8, ar.shape[1]), F32)
    for r in range(8):
        pw = pows[7 - r] if reverse else pows[r]
        tr = jnp.where(rows == r, pw[0], tr)
        ti = jnp.where(rows == r, pw[1], ti)
    return p1, p2, p4, p8, tr, ti


def _tile_scan(xr, xi, p1, p2, p4, reverse):
    rows = lax.broadcasted_iota(jnp.int32, xr.shape, 0)
    for s, (pr, pi) in ((1, p1), (2, p2), (4, p4)):
        if reverse:
            keep = rows < 8 - s
            sr, si = pltpu.roll(xr, 8 - s, 0), pltpu.roll(xi, 8 - s, 0)
        else:
            keep = rows >= s
            sr, si = pltpu.roll(xr, s, 0), pltpu.roll(xi, s, 0)
        sr, si = jnp.where(keep, sr, 0.0), jnp.where(keep, si, 0.0)
        mr, mi = _cmul(pr, pi, sr, si)
        xr, xi = xr + mr, xi + mi
    return xr, xi


def _s5_scan_fwd(bu, a, name, tb=512, comm=None):
    t = bu.shape[0]
    cb = SCAN_CB
    nt = t // tb

    def body(b_ref, a_ref, x_ref, carry):
        @pl.when(pl.program_id(1) == 0)
        def _():
            carry[...] = jnp.zeros_like(carry)

        ar, ai = a_ref[:, 0:cb], a_ref[:, cb:2 * cb]
        p1, p2, p4, p8, tr, ti = _scan_tables(ar, ai, False)

        def step(j, c):
            cr, ci = c
            i = pl.multiple_of(j * 8, 8)
            xr, xi = _tile_scan(b_ref[pl.ds(i, 8), 0:cb], b_ref[pl.ds(i, 8), cb:2 * cb], p1, p2, p4, False)
            mr, mi = _cmul(tr, ti, cr, ci)
            xr, xi = xr + mr, xi + mi
            x_ref[pl.ds(i, 8), 0:cb] = xr
            x_ref[pl.ds(i, 8), cb:2 * cb] = xi
            return xr[7:8, :], xi[7:8, :]

        cr, ci = lax.fori_loop(0, tb // 8, step, (carry[0:1, :], carry[1:2, :]), unroll=2)
        carry[0:1, :] = cr
        carry[1:2, :] = ci

    blk = pl.BlockSpec((tb, 2 * cb), lambda j, i: (i, j))
    return _call(
        body, name=name, grid=(SSM_CH // cb, nt),
        in_specs=[blk, pl.BlockSpec((1, 2 * cb), lambda j, i: (0, j))], out_specs=blk,
        out_shape=jax.ShapeDtypeStruct((t, 2 * SSM_CH), F32), scratch_shapes=[pltpu.VMEM((8, cb), F32)],
        sem=("parallel", "arbitrary"), args=(bu, a), comm=comm)


def _s5_scan_bwd(dx, x, a, name, tb=512, comm=None):
    t = dx.shape[0]
    cb = SCAN_CB
    nt = t // tb
    nj = tb // 8

    def body(d_ref, x_ref, xp_ref, a_ref, l_ref, da_ref, carry, acc):
        tblk = pl.program_id(1)

        @pl.when(tblk == 0)
        def _():
            carry[...] = jnp.zeros_like(carry)
            acc[...] = jnp.zeros_like(acc)

        ar, ai = a_ref[:, 0:cb], a_ref[:, cb:2 * cb]
        p1, p2, p4, p8, tr, ti = _scan_tables(ar, -ai, True)
        rows = lax.broadcasted_iota(jnp.int32, (8, cb), 0)

        def step(jj, c):
            cr, ci, sr_acc, si_acc = c
            j = nj - 1 - jj
            i = pl.multiple_of(j * 8, 8)
            lr, li = _tile_scan(d_ref[pl.ds(i, 8), 0:cb], d_ref[pl.ds(i, 8), cb:2 * cb], p1, p2, p4, True)
            mr, mi = _cmul(tr, ti, cr, ci)
            lr, li = lr + mr, li + mi
            l_ref[pl.ds(i, 8), 0:cb] = lr
            l_ref[pl.ds(i, 8), cb:2 * cb] = li
            ip = pl.multiple_of(jnp.maximum(j - 1, 0) * 8, 8)
            prev_r = jnp.where(j > 0, x_ref[pl.ds(ip, 8), 0:cb], xp_ref[:, 0:cb])
            prev_i = jnp.where(j > 0, x_ref[pl.ds(ip, 8), cb:2 * cb], xp_ref[:, cb:2 * cb])
            edge = jnp.where(jnp.logical_and(j == 0, tblk == nt - 1), 0.0, 1.0)
            xs_r = jnp.where(rows == 0, pltpu.roll(prev_r, 1, 0) * edge, pltpu.roll(x_ref[pl.ds(i, 8), 0:cb], 1, 0))
            xs_i = jnp.where(rows == 0, pltpu.roll(prev_i, 1, 0) * edge, pltpu.roll(x_ref[pl.ds(i, 8), cb:2 * cb], 1, 0))
            sr_acc = sr_acc + lr * xs_r + li * xs_i
            si_acc = si_acc + li * xs_r - lr * xs_i
            return lr[0:1, :], li[0:1, :], sr_acc, si_acc

        cr, ci, sr_acc, si_acc = lax.fori_loop(
            0, nj, step, (carry[0:1, :], carry[1:2, :], acc[:, 0:cb], acc[:, cb:2 * cb]))
        carry[0:1, :] = cr
        carry[1:2, :] = ci
        acc[:, 0:cb] = sr_acc
        acc[:, cb:2 * cb] = si_acc

        @pl.when(tblk == nt - 1)
        def _():
            da_ref[...] = jnp.sum(acc[...], axis=0, keepdims=True)

    blk = pl.BlockSpec((tb, 2 * cb), lambda j, i: (nt - 1 - i, j))
    prev = pl.BlockSpec((8, 2 * cb), lambda j, i: (jnp.maximum((nt - 1 - i) * (tb // 8) - 1, 0), j))
    vec = pl.BlockSpec((1, 2 * cb), lambda j, i: (0, j))
    return _call(
        body, name=name, grid=(SSM_CH // cb, nt), in_specs=[blk, blk, prev, vec], out_specs=[blk, vec],
        out_shape=[jax.ShapeDtypeStruct((t, 2 * SSM_CH), F32), jax.ShapeDtypeStruct((1, 2 * SSM_CH), F32)],
        scratch_shapes=[pltpu.VMEM((8, cb), F32), pltpu.VMEM((8, 2 * cb), F32)],
        sem=("parallel", "arbitrary"), args=(dx, x, x, a), comm=comm)


def _glu_fwd(yc, u, dvec, wg, bg, name, tr=256):
    t = yc.shape[0]

    def body(yc_ref, u_ref, d_ref, w_ref, b_ref, yl_ref, yb_ref):
        yl = yc_ref[...] + d_ref[...] * u_ref[...]
        yl_ref[...] = yl
        yg, _ = _gelu_and_grad(yl)
        z = jnp.dot(yg.astype(BF16), w_ref[...], preferred_element_type=F32) + b_ref[...]
        yb_ref[...] = (yg * _sigmoid(z)).astype(BF16)

    blk = pl.BlockSpec((tr, SSM_WIDTH), lambda i: (i, 0))
    vec = pl.BlockSpec((1, SSM_WIDTH), lambda i: (0, 0))
    return pl.pallas_call(
        body, name=name, grid=(t // tr,),
        in_specs=[blk, blk, vec, pl.BlockSpec((SSM_WIDTH, SSM_WIDTH), lambda i: (0, 0)), vec],
        out_specs=[blk, blk],
        out_shape=[jax.ShapeDtypeStruct((t, SSM_WIDTH), F32), jax.ShapeDtypeStruct((t, SSM_WIDTH), BF16)],
        compiler_params=_params(("parallel",)))(yc, u, dvec, wg, bg)


def _glu_bwd(yl, u, dvec, wg, bg, dyb, name, tr=256):
    t = yl.shape[0]

    def body(yl_ref, u_ref, d_ref, w_ref, b_ref, dy_ref, dyl_ref, du_ref, dw_ref, db_ref, dd_ref):
        @pl.when(pl.program_id(0) == 0)
        def _():
            dw_ref[...] = jnp.zeros_like(dw_ref)
            db_ref[...] = jnp.zeros_like(db_ref)
            dd_ref[...] = jnp.zeros_like(dd_ref)

        ylv, dyv, wv = yl_ref[...], dy_ref[...].astype(F32), w_ref[...]
        yg, dgelu = _gelu_and_grad(ylv)
        ygb = yg.astype(BF16)
        z = jnp.dot(ygb, wv, preferred_element_type=F32) + b_ref[...]
        sg = _sigmoid(z)
        dz = dyv * yg * sg * (1.0 - sg)
        dzb = dz.astype(BF16)
        dyg = dyv * sg + lax.dot_general(dzb, wv, (((1,), (1,)), ((), ())), preferred_element_type=F32)
        dyl = dyg * dgelu
        dyl_ref[...] = dyl.astype(BF16)
        du_ref[...] = dyl * d_ref[...]
        dw_ref[...] += lax.dot_general(ygb, dzb, (((0,), (0,)), ((), ())), preferred_element_type=F32)
        db_ref[...] += jnp.sum(dz, axis=0, keepdims=True)
        dd_ref[...] += jnp.sum(dyl * u_ref[...], axis=0, keepdims=True)

    blk = pl.BlockSpec((tr, SSM_WIDTH), lambda i: (i, 0))
    vec = pl.BlockSpec((1, SSM_WIDTH), lambda i: (0, 0))
    wsp = pl.BlockSpec((SSM_WIDTH, SSM_WIDTH), lambda i: (0, 0))
    return pl.pallas_call(
        body, name=name, grid=(t // tr,), in_specs=[blk, blk, vec, wsp, vec, blk],
        out_specs=[blk, blk, wsp, vec, vec],
        out_shape=[jax.ShapeDtypeStruct((t, SSM_WIDTH), BF16), jax.ShapeDtypeStruct((t, SSM_WIDTH), F32),
                   jax.ShapeDtypeStruct((SSM_WIDTH, SSM_WIDTH), F32), jax.ShapeDtypeStruct((1, SSM_WIDTH), F32),
                   jax.ShapeDtypeStruct((1, SSM_WIDTH), F32)],
        compiler_params=_params(("arbitrary",)))(yl, u, dvec, wg, bg, dyb)


def _mesh_pos():
    return lax.axis_index("x"), lax.axis_index("y"), lax.axis_index("c")


def _device_index():
    x, y, c = _mesh_pos()
    return 4 * x + 2 * y + c


def _gather_comm(arrays):
    na = len(arrays)

    def own_copy(ins, outs, sems, ai):
        return pltpu.make_async_copy(ins[ai], outs[ai].at[_device_index()], sems[2].at[ai])

    def ctx(ins, outs, sems):
        send_sems, recv_sems = sems[:2]
        x, y, c = _mesh_pos()
        chips = [(1 - x, y), (x, 1 - y), (1 - x, 1 - y)]

        def copy(ai, kk, block, to, own=False):
            slot = outs[ai].at[4 * block[0] + 2 * block[1] + block[2]]
            return pltpu.make_async_remote_copy(
                src_ref=ins[ai] if own else slot, dst_ref=slot, send_sem=send_sems.at[ai, kk],
                recv_sem=recv_sems.at[ai, kk], device_id=to, device_id_type=MESH)

        return (x, y, c), (x, y, 1 - c), chips, c, copy

    def start(ins, outs, sems):
        me, sibling, chips, c, copy = ctx(ins, outs, sems)
        for ai in range(na):
            copy(ai, 0, me, sibling, own=True).start()
            for j, chip in enumerate(chips):
                copy(ai, 1 + j, me, (*chip, c), own=True).start()
        for ai in range(na):
            own_copy(ins, outs, sems, ai).start()

    def mid(ins, outs, sems):
        me, sibling, chips, c, copy = ctx(ins, outs, sems)
        for ai in range(na):
            for j, chip in enumerate(chips):
                copy(ai, 1 + j, (*chip, c), me).wait_recv()
                copy(ai, 4 + j, (*chip, c), sibling).start()

    def end(ins, outs, sems):
        me, sibling, chips, c, copy = ctx(ins, outs, sems)
        for ai in range(na):
            copy(ai, 0, sibling, me).wait_recv()
            copy(ai, 0, me, sibling, own=True).wait_send()
            for j, chip in enumerate(chips):
                copy(ai, 4 + j, (*chip, 1 - c), me).wait_recv()
                copy(ai, 1 + j, me, (*chip, c), own=True).wait_send()
                copy(ai, 4 + j, (*chip, c), sibling).wait_send()
            own_copy(ins, outs, sems, ai).wait()

    return Comm(arrays, [jax.ShapeDtypeStruct((N_DEV,) + a.shape, a.dtype) for a in arrays],
                [pltpu.SemaphoreType.DMA((na, 7)), pltpu.SemaphoreType.DMA((na, 7)), pltpu.SemaphoreType.DMA((na,))],
                start, end, mid)


def _sequencer_gather(arrays, name, collective_id):
    comm = _gather_comm(arrays)
    na = len(arrays)

    def body(*refs):
        ins, outs, sems = refs[:na], refs[na:2 * na], refs[2 * na:]
        x, y, c = _mesh_pos()
        peers = [(x, y, 1 - c), (1 - x, y, c), (x, 1 - y, c), (1 - x, 1 - y, c)]
        barrier = pltpu.get_barrier_semaphore()
        for peer in peers:
            pl.semaphore_signal(barrier, inc=1, device_id=peer, device_id_type=MESH)
        pl.semaphore_wait(barrier, len(peers))
        comm.start(ins, outs, sems)
        comm.mid(ins, outs, sems)
        comm.end(ins, outs, sems)

    return list(pl.kernel(
        body, out_type=tuple(comm.out_shapes), mesh=plsc.ScalarSubcoreMesh(axis_name="sequencer", num_cores=1),
        name=name, scratch_types=tuple(comm.sems),
        compiler_params=pltpu.CompilerParams(collective_id=collective_id))(*arrays))


def _sequencer_exchange(comm, peers_of, name, collective_id):
    na = len(comm.inputs)

    def body(*refs):
        ins, outs, sems = refs[:na], refs[na:na + len(comm.out_shapes)], refs[na + len(comm.out_shapes):]
        peers = peers_of(*_mesh_pos())
        barrier = pltpu.get_barrier_semaphore()
        for peer in peers:
            pl.semaphore_signal(barrier, inc=1, device_id=peer, device_id_type=MESH)
        pl.semaphore_wait(barrier, len(peers))
        comm.start(ins, outs, sems)
        comm.end(ins, outs, sems)

    return list(pl.kernel(
        body, out_type=tuple(comm.out_shapes), mesh=plsc.ScalarSubcoreMesh(axis_name="sequencer", num_cores=1),
        name=name, scratch_types=tuple(comm.sems),
        compiler_params=pltpu.CompilerParams(collective_id=collective_id))(*comm.inputs))


SIBLING_SWAP_ID, CHIP_EXCHANGE_ID = 9, 10


def _sequencer_swap(arrays, name):
    return _sequencer_exchange(_swap_comm(arrays), lambda x, y, c: [(x, y, 1 - c)], name, SIBLING_SWAP_ID)[0]


def _sequencer_chips(send, name):
    return _sequencer_exchange(_chips_comm(send), lambda x, y, c: [(1 - x, y, c), (x, 1 - y, c), (1 - x, 1 - y, c)],
                               name, CHIP_EXCHANGE_ID)[0]


def _swap_comm(arrays):
    na = len(arrays)
    offs = np.concatenate([[0], np.cumsum([a.shape[1] for a in arrays])]).astype(int)

    def copies(ins, outs, sems):
        x, y, c = _mesh_pos()
        return [pltpu.make_async_remote_copy(
            src_ref=ins[ai].at[2 * k + 1 - c], dst_ref=outs[0].at[k, pl.ds(int(offs[ai]), arrays[ai].shape[1])],
            send_sem=sems[0].at[ai, k], recv_sem=sems[1].at[ai, k], device_id=(x, y, 1 - c), device_id_type=MESH)
            for ai in range(na) for k in range(4)]

    def start(ins, outs, sems):
        for cp in copies(ins, outs, sems):
            cp.start()

    def end(ins, outs, sems):
        for cp in copies(ins, outs, sems):
            cp.wait()

    return Comm(arrays, [jax.ShapeDtypeStruct((4, int(offs[-1]), PACK_COLS), arrays[0].dtype)],
                [pltpu.SemaphoreType.DMA((na, 4)), pltpu.SemaphoreType.DMA((na, 4))], start, end)


def _chips_comm(send):
    def copies(ins, outs, sems):
        x, y, c = _mesh_pos()
        chips = [(1 - x, y), (x, 1 - y), (1 - x, 1 - y)]
        return [pltpu.make_async_remote_copy(
            src_ref=ins[0].at[2 * cx + cy], dst_ref=outs[0].at[j], send_sem=sems[0].at[j], recv_sem=sems[1].at[j],
            device_id=(cx, cy, c), device_id_type=MESH) for j, (cx, cy) in enumerate(chips)]

    def start(ins, outs, sems):
        for cp in copies(ins, outs, sems):
            cp.start()

    def end(ins, outs, sems):
        for cp in copies(ins, outs, sems):
            cp.wait()

    return Comm([send], [jax.ShapeDtypeStruct((3,) + send.shape[1:], send.dtype)],
                [pltpu.SemaphoreType.DMA((3,)), pltpu.SemaphoreType.DMA((3,))], start, end)


def _pair_sum(keep, recv, name, tr=464):
    nchip, rows, cols = keep.shape

    def body(g_ref, r_ref, o_ref):
        o_ref[...] = (g_ref[...].astype(F32) + r_ref[...].astype(F32)).astype(BF16)

    blk = pl.BlockSpec((1, tr, cols), lambda k, i: (k, i, 0))
    return pl.pallas_call(
        body, name=name, grid=(nchip, rows // tr), in_specs=[blk, blk], out_specs=blk,
        out_shape=jax.ShapeDtypeStruct((nchip, rows, cols), BF16),
        compiler_params=_params(("parallel", "parallel")))(keep, recv)


def _pair_sum_pieces(pieces, recv, name, tr):
    _, rows, cols = pieces.shape
    core = lax.axis_index("c").astype(jnp.int32).reshape(1)

    def body(c_ref, g_ref, r_ref, o_ref):
        del c_ref
        o_ref[...] = (g_ref[...].astype(F32) + r_ref[...].astype(F32)).astype(BF16)

    grid_spec = pltpu.PrefetchScalarGridSpec(
        num_scalar_prefetch=1, grid=(4, rows // tr),
        in_specs=[pl.BlockSpec((1, tr, cols), lambda k, i, c_ref: (2 * k + c_ref[0], i, 0)),
                  pl.BlockSpec((1, tr, cols), lambda k, i, c_ref: (k, i, 0))],
        out_specs=pl.BlockSpec((1, tr, cols), lambda k, i, c_ref: (k, i, 0)))
    return pl.pallas_call(
        body, name=name, grid_spec=grid_spec, out_shape=jax.ShapeDtypeStruct((4, rows, cols), BF16),
        compiler_params=_params(("parallel", "parallel")))(core, pieces, recv)


def _chip_sum(own, others, name, tr=464):
    _, rows, cols = own.shape
    chip = (2 * lax.axis_index("x") + lax.axis_index("y")).astype(jnp.int32).reshape(1)

    def body(chip_ref, own_ref, oth_ref, o_ref):
        del chip_ref
        acc = own_ref[0].astype(F32)
        for j in range(3):
            acc = acc + oth_ref[j].astype(F32)
        o_ref[...] = acc

    grid_spec = pltpu.PrefetchScalarGridSpec(
        num_scalar_prefetch=1, grid=(rows // tr,),
        in_specs=[pl.BlockSpec((1, tr, cols), lambda i, chip_ref: (chip_ref[0], i, 0)),
                  pl.BlockSpec((3, tr, cols), lambda i, chip_ref: (0, i, 0))],
        out_specs=pl.BlockSpec((tr, cols), lambda i, chip_ref: (i, 0)))
    return pl.pallas_call(
        body, name=name, grid_spec=grid_spec, out_shape=jax.ShapeDtypeStruct((rows, cols), F32),
        compiler_params=_params(("parallel",)))(chip, own, others)


def _sum_leading(parts, name, tr=464):
    nparts, rows, cols = parts.shape
    tr = tr if rows % tr == 0 else rows

    def body(p_ref, o_ref):
        acc = p_ref[0].astype(F32)
        for i in range(1, nparts):
            acc = acc + p_ref[i].astype(F32)
        o_ref[...] = acc

    return pl.pallas_call(
        body, name=name, grid=(rows // tr,),
        in_specs=[pl.BlockSpec((nparts, tr, cols), lambda i: (0, i, 0))],
        out_specs=pl.BlockSpec((tr, cols), lambda i: (i, 0)), out_shape=jax.ShapeDtypeStruct((rows, cols), F32),
        compiler_params=_params(("parallel",)))(parts)


def _adamw(w, g, m, v, name, comm=None):
    shape = w.shape
    cols = shape[-1]
    lead = shape[0] if len(shape) >= 3 else 1
    rows = int(np.prod(shape[:-1])) // lead if len(shape) > 1 else 1
    w2, g2, m2, v2 = (a.reshape(lead, rows, cols) for a in (w, g, m, v))
    tr = rows
    for cand in (512, 256, 128, 64, 32, 16, 8):
        if rows % cand == 0 and rows > cand:
            tr = cand
            break
    bc1, bc2 = 1.0 - ADAM_B1 ** ADAM_STEP, 1.0 - ADAM_B2 ** ADAM_STEP

    def body(w_ref, g_ref, m_ref, v_ref, d_ref, nm_ref, nv_ref):
        gv = g_ref[...]
        nm = ADAM_B1 * m_ref[...] + (1.0 - ADAM_B1) * gv
        nv = ADAM_B2 * v_ref[...] + (1.0 - ADAM_B2) * (gv * gv)
        nm_ref[...] = nm
        nv_ref[...] = nv
        d_ref[...] = -ADAM_LR * ((nm / bc1) / (jnp.sqrt(nv / bc2) + ADAM_EPS) + ADAM_WD * w_ref[...])

    blk = pl.BlockSpec((1, tr, cols), lambda l, i: (l, i, 0))
    res = _call(body, name=name, grid=(lead, rows // tr), in_specs=[blk] * 4, out_specs=[blk] * 3,
                out_shape=[jax.ShapeDtypeStruct((lead, rows, cols), F32)] * 3, sem=("parallel", "parallel"),
                args=(w2, g2, m2, v2), comm=comm)
    outs, couts = res if comm is not None else (res, None)
    outs = tuple(o.reshape(shape) for o in outs)
    return outs if comm is None else (outs, couts)


WEIGHT_NAMES = ['norm_mix_g', 'norm_xa_g', 'norm_ffn_g', 'norm_mem_g', 'norm_final_g', 'w_in_ab', 'conv_qkv_a',
                'a_log_a', 'dt_bias_a', 'onorm_g_a', 'ssm_lambda_re', 'ssm_lambda_im', 'ssm_b_re', 'ssm_b_im',
                'ssm_c_re', 'ssm_c_im', 'ssm_d', 'ssm_log_dt', 'w_glu_b', 'b_glu_b', 'w_out_ab', 'pool_w',
                'pool_scale', 'xa_wq', 'xa_wkv', 'xa_wo', 'ffn_w_up', 'ffn_conv', 'ffn_w_down']
BIG_SHARDED = {'w_in_ab': ((1, 1024, 2568), 2), 'w_glu_b': ((1, 512, 512), 1), 'w_out_ab': ((1, 1024, 1024), 1),
               'pool_w': ((1, 4, 256, 256), 2), 'xa_wq': ((2, 1024, 1024), 1), 'xa_wkv': ((2, 1024, 2048), 2),
               'xa_wo': ((2, 1024, 1024), 1), 'ffn_w_up': ((2, 1024, 5632), 2), 'ffn_w_down': ((2, 2816, 1024), 1)}
SMALL_SHARDED = {'conv_qkv_a': ((1, 4, 1536), 2), 'pool_scale': ((1, 1024), 1), 'ffn_conv': ((2, 3, 5632), 2)}
REPLICATED = {'norm_mix_g': (2, 1024), 'norm_xa_g': (2, 1024), 'norm_ffn_g': (2, 1024), 'norm_mem_g': (1024,),
              'norm_final_g': (1024,), 'a_log_a': (1, 4), 'dt_bias_a': (1, 4), 'onorm_g_a': (1, 128),
              'ssm_lambda_re': (1, 32, 64), 'ssm_lambda_im': (1, 32, 64), 'ssm_b_re': (1, 32, 64, 16),
              'ssm_b_im': (1, 32, 64, 16), 'ssm_c_re': (1, 32, 16, 64), 'ssm_c_im': (1, 32, 16, 64),
              'ssm_d': (1, 32, 16), 'ssm_log_dt': (1, 32), 'b_glu_b': (1, 512)}
PACK_ROW_ALIGN = 8


def _shard_shape(shape, axis):
    return tuple(s // N_DEV if i == axis else s for i, s in enumerate(shape))


def _round_up(n, m):
    return (n + m - 1) // m * m


def _pack(arrays):
    total = sum(int(np.prod(a.shape)) for a in arrays)
    padded = _round_up(total, PACK_COLS * PACK_ROW_ALIGN)
    parts = [a.astype(F32).reshape(-1) for a in arrays]
    if padded != total:
        parts.append(jnp.zeros((padded - total,), F32))
    return jnp.concatenate(parts).reshape(padded // PACK_COLS, PACK_COLS)


def _unpack(packed, shapes):
    flat, out, off = packed.reshape(-1), [], 0
    for shape in shapes:
        size = int(np.prod(shape))
        out.append(flat[off:off + size].reshape(shape))
        off += size
    return out


def _split_shards(full, axis):
    shape = full.shape
    s = shape[axis] // N_DEV
    a = full.reshape(shape[:axis] + (N_DEV, s) + shape[axis + 1:])
    return jnp.moveaxis(a, axis, 0).reshape(N_DEV, -1)


def _merge_shards(pieces, shape, axis):
    sh = _shard_shape(shape, axis)
    a = pieces.reshape((N_DEV,) + sh)
    a = jnp.moveaxis(a, 0, axis)
    return a.reshape(shape)


_SCAN_NB = SSM_CH // SCAN_CB


def _to_scan_layout(m, axis):
    shape = m.shape
    m = m.reshape(shape[:axis] + (2, _SCAN_NB, SCAN_CB) + shape[axis + 1:])
    return jnp.swapaxes(m, axis, axis + 1).reshape(shape)


def _from_scan_layout(m, axis):
    shape = m.shape
    m = m.reshape(shape[:axis] + (_SCAN_NB, 2, SCAN_CB) + shape[axis + 1:])
    return jnp.swapaxes(m, axis, axis + 1).reshape(shape)


def _s5_discretise(lam_re, lam_im, b_re, b_im, log_dt):
    dt = jnp.exp(log_dt)[:, None]
    mag = jnp.exp(lam_re * dt)
    ang = lam_im * dt
    lb_re, lb_im = mag * jnp.cos(ang), mag * jnp.sin(ang)
    den = lam_re * lam_re + lam_im * lam_im
    nr, ni = lb_re - 1.0, lb_im
    coef_re = (nr * lam_re + ni * lam_im) / den
    coef_im = (ni * lam_re - nr * lam_im) / den
    bb_re = coef_re[..., None] * b_re - coef_im[..., None] * b_im
    bb_im = coef_re[..., None] * b_im + coef_im[..., None] * b_re
    return lb_re, lb_im, bb_re, bb_im


_GROUPS_PER_BLOCK = N_GROUPS // _SCAN_NB
_U_BLOCK = _GROUPS_PER_BLOCK * SSM_GROUP


def _s5_matrices(lb_re, lb_im, bb_re, bb_im, c_re, c_im):
    eye = jnp.eye(_GROUPS_PER_BLOCK, dtype=F32)
    blocked = lambda m: m.reshape((_SCAN_NB, _GROUPS_PER_BLOCK) + m.shape[1:])
    bmat = lambda bb: jnp.einsum('jgph,gk->jghkp', blocked(bb), eye).reshape(_SCAN_NB, _U_BLOCK, SCAN_CB)
    cmat = lambda cc: jnp.einsum('jghp,gk->jkpgh', blocked(cc), eye).reshape(_SCAN_NB, SCAN_CB, _U_BLOCK)
    b_in = jnp.concatenate([bmat(bb_re), bmat(bb_im)], axis=2)
    c_out = jnp.concatenate([cmat(c_re), -cmat(c_im)], axis=1)
    a_row = _to_scan_layout(jnp.concatenate([lb_re.reshape(1, SSM_CH), lb_im.reshape(1, SSM_CH)], axis=1), 1)
    return b_in, c_out, a_row


def _s5_matrix_grads(db_in, dc_out, da_row):
    da_nat = _from_scan_layout(da_row, 1)
    eye = jnp.eye(_GROUPS_PER_BLOCK, dtype=F32)
    nb, gb = _SCAN_NB, _GROUPS_PER_BLOCK
    bgrad = lambda m: jnp.einsum('jghkp,gk->jgph', m.reshape(nb, gb, SSM_GROUP, gb, SSM_STATE), eye
                                 ).reshape(N_GROUPS, SSM_STATE, SSM_GROUP)
    cgrad = lambda m: jnp.einsum('jkpgh,gk->jghp', m.reshape(nb, gb, SSM_STATE, gb, SSM_GROUP), eye
                                 ).reshape(N_GROUPS, SSM_GROUP, SSM_STATE)
    dbb_re, dbb_im = bgrad(db_in[:, :, :SCAN_CB]), bgrad(db_in[:, :, SCAN_CB:])
    dc_re, dc_im = cgrad(dc_out[:, :SCAN_CB]), -cgrad(dc_out[:, SCAN_CB:])
    dlb_re = da_nat[0, :SSM_CH].reshape(N_GROUPS, SSM_STATE)
    dlb_im = da_nat[0, SSM_CH:].reshape(N_GROUPS, SSM_STATE)
    return dlb_re, dlb_im, dbb_re, dbb_im, dc_re, dc_im


def _as_pieces(a):
    return a.reshape(N_DEV, a.shape[0] // N_DEV, a.shape[1])


def _hybrid_fwd(xn, x, wts, p, weights, riders):
    sv = {}
    hq = _mm(xn, wts['w_qkv_t'], "nt", "l0_in_qkv")
    gate = _mm(xn, wts['w_gate_t'], "nt", "l0_in_gate")
    ba = _mm(xn, wts['w_ba_t'], "nt", "l0_in_ba")
    u = _mm(xn, wts['w_u_t'], "nt", "l0_in_u")
    conv = p['conv_qkv']
    qkv = _qkv_pre_fwd(hq, conv, "l0_qkv_pre")
    gates = _gates_fwd(ba, p['arow'], p['brow'], "l0_gates")
    o, tm_all, s_all = riders.run("l0_gdr_fwd", _gdr_fwd, qkv, gates)
    wts['w_glu'], wts['w_out'] = weights.full['w_glu'], weights.full['w_out']
    y_a = _onorm_fwd(o, gate, p['onorm_g'], "l0_onorm")
    bu = riders.run("l0_s5_bu", _mm_bd, u, p['b_in'], "nn")
    xs = riders.run("l0_s5_scan", _s5_scan_fwd, bu, p['a_row'])
    weights.gather_by_sequencer(GATHER_LAYER1, xs, "gather_layer1", GATHER_LAYER1_ID)
    yc = riders.run("l0_s5_cx", _mm_bd, xs, p['c_out'], "nn")
    yl, y_b = _glu_fwd(yc, u, p['d_row'], wts['w_glu'], p['b_glu'], "l0_glu")
    mixed = jnp.concatenate([y_a, y_b], axis=1)
    x1 = _mm(mixed, wts['w_out'], "nn", "l0_out", res=x)
    sv.update(hq=hq, gate=gate, ba=ba, u=u, qkv=qkv, gb=gates, o=o, tm=tm_all, s=s_all, xs=xs, yl=yl, mixed=mixed)
    return x1, sv


def _hybrid_bwd(dx1, xn, wts, p, sv, riders):
    gr = {}
    dmixed = _mm(dx1, wts['w_out'], "nt", "l0_out_dx", out_dtype=BF16)
    riders.grad('w_out', _as_pieces(_mm(sv['mixed'], dx1, "tn", "l0_out_dw", out_dtype=BF16)))
    dya, dyb = dmixed[:, :WIDTH_A], dmixed[:, WIDTH_A:]
    dyl, du_direct, dw_glu, gr['b_glu_b'], dd = _glu_bwd(
        sv['yl'], sv['u'], p['d_row'], wts['w_glu'], p['b_glu'], dyb, "l0_glu_bwd")
    riders.grad('w_glu', dw_glu.astype(BF16).reshape(N_DEV, -1, PACK_COLS))
    dxs = riders.run("l0_s5_cx_dx", _mm_bd, dyl, p['c_out'], "nt")
    dc_out = _mm_bd(sv['xs'], dyl, "tn", "l0_s5_cx_dw")
    lam, da_row = riders.run("l0_s5_scan_bwd", _s5_scan_bwd, dxs, sv['xs'], p['a_row'])
    du = _mm_bd(lam, p['b_in'], "nt", "l0_s5_bu_dx", res=du_direct, out_dtype=BF16)
    db_in = _mm_bd(sv['u'], lam, "tn", "l0_s5_bu_dw")
    gr['s5'] = (db_in, dc_out, da_row, dd)
    do, dgate, gr['onorm_g_a'] = _onorm_bwd(sv['o'], sv['gate'], p['onorm_g'], dya, "l0_onorm_bwd")
    dqkv, dgb = riders.run("l0_gdr_bwd", _gdr_bwd, sv['qkv'], sv['gb'], sv['tm'], sv['s'], do)
    dhq, gr['conv_qkv_a'] = _qkv_pre_bwd(sv['hq'], p['conv_qkv'], dqkv, "l0_qkv_pre_bwd")
    dba, da_log, ddt_bias = _gates_bwd(sv['ba'], p['arow'], p['brow'], dgb, "l0_gates_bwd")
    gr['a_log_a'], gr['dt_bias_a'] = da_log[:, 4:8], ddt_bias[:, 4:8]
    dw_qkv_t = _mm(dhq, xn, "tn", "l0_in_qkv_dw", out_dtype=BF16)
    dw_gate_t = _mm(dgate, xn, "tn", "l0_in_gate_dw", out_dtype=BF16)
    dw_ba_t = _mm(dba, xn, "tn", "l0_in_ba_dw", out_dtype=BF16)
    dw_u_t = _mm(du, xn, "tn", "l0_in_u_dw", out_dtype=BF16)
    dw_in_t = _as_pieces(jnp.concatenate([dw_qkv_t, dw_gate_t, dw_ba_t[:8], dw_u_t], axis=0))
    riders.grad('w_in_t', jnp.concatenate(
        [dw_in_t, jnp.zeros((N_DEV, dict(PIECES)['w_in_t'] - W_IN_PIECE, D_MODEL), BF16)], axis=1))
    dxn = riders.run("l0_in_qkv_dx", _mm, dhq, wts['w_qkv_t'], "nn")
    dxn = _mm(dgate, wts['w_gate_t'], "nn", "l0_in_gate_dx", res=dxn)
    dxn = _mm(dba, wts['w_ba_t'], "nn", "l0_in_ba_dx", res=dxn)
    dxn = riders.run("l0_in_u_dx", _mm, du, wts['w_u_t'], "nn", res=dxn)
    return dxn, gr


def _xa_fwd(x1, g, mem_n, wq, wkv_t, wo, tag, riders):
    xq = _rms_fwd(x1, g, BF16, tag + "_norm")
    q = _mm(xq, wq, "nn", tag + "_q", out_dtype=BF16)
    kv = _mm(mem_n, wkv_t, "nt", tag + "_kv", out_dtype=BF16)
    o = riders.run(tag + "_attn", _attn_fwd, q, kv)
    x2 = _mm(o, wo, "nn", tag + "_o", res=x1)
    return x2, dict(xq=xq, q=q, kv=kv, o=o)


def _xa_bwd(dx2, x1, g, mem_n, wq, wkv_t, wo, sv, tag, layer, riders):
    do = _mm(dx2, wo, "nt", tag + "_o_dx", out_dtype=BF16)
    riders.grad('wo%d' % layer, _as_pieces(_mm(sv['o'], dx2, "tn", tag + "_o_dw", out_dtype=BF16)))
    dq, dk, dv = _attn_bwd(sv['q'], sv['kv'], do, tag + "_attn_bwd")
    dkv = jnp.concatenate([dk, dv], axis=1).astype(BF16)
    dxq = _mm(dq, wq, "nt", tag + "_q_dx")
    riders.grad('wq%d' % layer, _as_pieces(_mm(sv['xq'], dq, "tn", tag + "_q_dw", out_dtype=BF16)))
    dmem_n = _mm(dkv, wkv_t, "nn", tag + "_kv_dx")
    riders.grad('wkv_t%d' % layer, _as_pieces(_mm(dkv, mem_n, "tn", tag + "_kv_dw", out_dtype=BF16)))
    dx1, dg = riders.run(tag + "_norm_bwd", _rms_bwd, x1, g, dxq, dx2)
    return dx1, dmem_n, dg


def _ffn_fwd(x2, g, w_up_t, conv, w_down, tag, riders):
    xf = _rms_fwd(x2, g, BF16, tag + "_norm")
    h = riders.run(tag + "_up", _mm, xf, w_up_t, "nt")
    a = riders.run(tag + "_act", _ffn_act_fwd, h, conv)
    x3 = _mm(a, w_down, "nn", tag + "_down", res=x2)
    return x3, dict(xf=xf, h=h, a=a)


def _ffn_bwd(dx3, x2, g, w_up_t, conv, w_down, sv, tag, layer, riders):
    da = _mm(dx3, w_down, "nt", tag + "_down_dx")
    riders.grad('down%d' % layer, _as_pieces(_mm(sv['a'], dx3, "tn", tag + "_down_dw", out_dtype=BF16)))
    dh, dconv = riders.run(tag + "_act_bwd", _ffn_act_bwd, sv['h'], conv, da)
    dxf = riders.run(tag + "_up_dx", _mm_parts, dh, w_up_t, "nn")
    dw_up_t = riders.run(tag + "_up_dw", _mm_parts, dh, sv['xf'], "tn", out_dtype=BF16)
    riders.grad('up_t%d' % layer, _as_pieces(dw_up_t))
    dx2, dg = riders.run(tag + "_norm_bwd", _rms_bwd, x2, g, dxf, dx3)
    return dx2, dconv, dg


BIG_NAMES, SMALL_NAMES, REP_NAMES = list(BIG_SHARDED), list(SMALL_SHARDED), list(REPLICATED)
SMALL_SIZES = [int(np.prod(_shard_shape(*SMALL_SHARDED[n]))) for n in SMALL_NAMES]


PIECES = [('w_in_t', 384), ('w_glu', 32), ('w_out', 128), ('pool_w', 32), ('wq0', 128), ('wq1', 128),
          ('wkv_t0', 256), ('wkv_t1', 256), ('wo0', 128), ('wo1', 128), ('up_t0', 704), ('up_t1', 704),
          ('down0', 352), ('down1', 352)]
W_IN_ROWS = 4 * WIDTH_A + 2 * N_HEADS_A + SSM_WIDTH
W_IN_PIECE = W_IN_ROWS // N_DEV


def _row_tile(rows):
    return max(t for t in range(16, min(rows, 512) + 1, 16) if rows % t == 0)


class _Riders:
    def __init__(self):
        self.waiting = {}
        self.deferred = {}
        self.grads = {}
        self.groups = []
        self.reduced = {}

    def add(self, host, comm, then):
        self.waiting.setdefault(host, []).append((comm, then))

    def after(self, marker, then):
        self.deferred.setdefault(marker, []).append(then)

    def mark(self, name, out=None):
        for cont in self.deferred.pop(name, []):
            step = cont()
            if step is not None:
                values, then = step
                out, values = lax.optimization_barrier((out, values))
                then(values)
        return out

    def run(self, name, fn, *args, **kw):
        riders = self.waiting.pop(name, [])
        if not riders:
            out = fn(*args, name=name, **kw)
        else:
            out, couts = fn(*args, name=name, comm=[c for c, _ in riders], **kw)
            for (_, then), got in zip(riders, couts):
                then(got)
        return self.mark(name, out)

    def grad(self, key, pieces):
        self.grads[key] = pieces
        for group in [g for g in self.groups if all(k in self.grads for k in g[1])]:
            self.groups.remove(group)
            self._reduce(*group)

    def _reduce(self, name, keys, pair_marker, sum_marker):
        arrays = [self.grads[k] for k in keys]
        rows = sum(a.shape[1] for a in arrays)
        tile = _row_tile(rows)
        from_sibling = _sequencer_swap(arrays, name + "_to_sibling")

        def after_swap():
            if len(arrays) == 1:
                chip_sums = _pair_sum_pieces(arrays[0], from_sibling, name + "_pair_sum", tr=tile)
            else:
                core = lax.axis_index("c")
                keep = jnp.concatenate(
                    [lax.dynamic_index_in_dim(a.reshape(4, 2, a.shape[1], PACK_COLS), core, 1, keepdims=False)
                     for a in arrays], axis=1)
                chip_sums = _pair_sum(keep, from_sibling, name + "_pair_sum", tr=tile)

            def exchange_among_chips(chip_sums):
                from_chips = _sequencer_chips(chip_sums, name + "_to_chips")

                def store(total):
                    off = 0
                    for k, a in zip(keys, arrays):
                        self.reduced[k] = total[off:off + a.shape[1]]
                        off += a.shape[1]

                self.after(sum_marker, lambda: (
                    _chip_sum(chip_sums, from_chips, name + "_chip_sum", tr=tile), store))

            return chip_sums, exchange_among_chips

        self.after(pair_marker, after_swap)


class _Weights:
    def __init__(self, inp):
        bf = lambda a: a.astype(BF16)
        local = {'w_in_t': bf(inp['w_in_ab'][0]).T, 'w_glu': bf(inp['w_glu_b'][0]), 'w_out': bf(inp['w_out_ab'][0]),
                 'pool_w': bf(inp['pool_w'][0]),
                 'small': _pack([inp[n] for n in SMALL_NAMES])}
        for l in range(2):
            local['wq%d' % l] = bf(inp['xa_wq'][l])
            local['wkv_t%d' % l] = bf(inp['xa_wkv'][l]).T
            local['wo%d' % l] = bf(inp['xa_wo'][l])
            local['up_t%d' % l] = bf(inp['ffn_w_up'][l]).T
            local['down%d' % l] = bf(inp['ffn_w_down'][l])
        self.local, self.full = local, {}

    def plan(self, keys):
        return _gather_comm([self.local[k] for k in keys])

    def gather_by_sequencer(self, keys, after, name, collective_id):
        arrays = [self.local[k] for k in keys]
        tie = (after.reshape(-1)[0] * 0.0).astype(arrays[0].dtype)
        arrays[0] = arrays[0] + tie
        self.land(keys, _sequencer_gather(arrays, name, collective_id))

    def land(self, keys, gathered):
        for k, g in zip(keys, gathered):
            if k == 'small':
                off = 0
                for n, size in zip(SMALL_NAMES, SMALL_SIZES):
                    self.full[n] = _merge_shards(g.reshape(N_DEV, -1)[:, off:off + size], *SMALL_SHARDED[n])
                    off += size
            elif k == 'pool_w':
                self.full[k] = jnp.swapaxes(g, 0, 1).reshape(len(POOL_WINDOWS), POOL_GROUP, POOL_GROUP)
            else:
                self.full[k] = g.reshape(N_DEV * g.shape[1], g.shape[2])


GATHER_FIRST = ['w_in_t', 'small']
GATHER_LAYER0 = ['w_glu', 'w_out', 'wq0', 'wkv_t0', 'wo0', 'down0', 'up_t0']
GATHER_LAYER1 = ['pool_w', 'wq1', 'wkv_t1', 'wo1', 'up_t1', 'down1']
GATHER_LAYER0_ID, GATHER_LAYER1_ID = 7, 8
GRAD_RIDES = [('g_down1', ['down1'], 'l1_ffn_up_dx', 'l1_xa_norm_bwd'),
              ('g_up1', ['up_t1'], 'l1_xa_norm_bwd', 'l0_ffn_up_dx'),
              ('g_xa1', ['wq1', 'wkv_t1', 'wo1', 'pool_w'], 'l0_ffn_up_dx', 'l0_xa_norm_bwd'),
              ('g_down0', ['down0'], 'l0_ffn_up_dx', 'l0_s5_scan_bwd'),
              ('g_l0', ['up_t0', 'wq0', 'wkv_t0', 'wo0'], 'l0_s5_cx_dx', 'l0_in_u_dx'),
              ('g_out', ['w_out', 'w_glu'], 'l0_s5_scan_bwd', 'l0_in_u_dx'),
              ('g_in', ['w_in_t'], 'l0_in_u_dx', 'adamw_pool_w')]


def _local_step(inp):
    f32_of = lambda n: inp[n].astype(F32)
    weights = _Weights(inp)
    riders = _Riders()
    riders.groups = list(GRAD_RIDES)
    full = weights.full
    weights.land(GATHER_FIRST, _comm_only(weights.plan(GATHER_FIRST), "gather_first"))
    weights.gather_by_sequencer(GATHER_LAYER0, full['w_in_t'], "gather_layer0", GATHER_LAYER0_ID)
    w_in_t = full['w_in_t']
    wts0 = dict(w_qkv_t=w_in_t[:3 * WIDTH_A], w_gate_t=w_in_t[3 * WIDTH_A:4 * WIDTH_A],
                w_ba_t=jnp.concatenate([w_in_t[4 * WIDTH_A:4 * WIDTH_A + 8], jnp.zeros((LANE - 8, D_MODEL), BF16)], 0),
                w_u_t=w_in_t[4 * WIDTH_A + 8:])
    lb_disc, disc_vjp = jax.vjp(_s5_discretise, f32_of('ssm_lambda_re')[0], f32_of('ssm_lambda_im')[0],
                                f32_of('ssm_b_re')[0], f32_of('ssm_b_im')[0], f32_of('ssm_log_dt')[0])
    b_in, c_out, a_row = _s5_matrices(*lb_disc, f32_of('ssm_c_re')[0], f32_of('ssm_c_im')[0])
    zeros4 = jnp.zeros((1, 4), F32)
    p0 = dict(conv_qkv=full['conv_qkv_a'][0], onorm_g=f32_of('onorm_g_a'),
              arow=jnp.concatenate([zeros4, f32_of('a_log_a'), jnp.zeros((1, LANE - 8), F32)], 1),
              brow=jnp.concatenate([zeros4, f32_of('dt_bias_a'), jnp.zeros((1, LANE - 8), F32)], 1),
              b_in=b_in.astype(BF16), c_out=c_out.astype(BF16), a_row=a_row,
              d_row=f32_of('ssm_d').reshape(1, SSM_WIDTH), b_glu=f32_of('b_glu_b'))

    x0 = inp['x'][0]
    mem_n = _rms_fwd(inp['mem'][0], inp['norm_mem_g'], BF16, "mem_norm")
    xn0 = _rms_fwd(x0, inp['norm_mix_g'][0], BF16, "l0_mix_norm")
    x1, sv_mix0 = _hybrid_fwd(xn0, x0, wts0, p0, weights, riders)
    x2, sv_xa0 = _xa_fwd(x1, inp['norm_xa_g'][0], mem_n, full['wq0'], full['wkv_t0'], full['wo0'], "l0_xa", riders)
    x3, sv_ffn0 = _ffn_fwd(x2, inp['norm_ffn_g'][0], full['up_t0'], full['ffn_conv'][0], full['down0'], "l0_ffn", riders)
    xn1 = _rms_fwd(x3, inp['norm_mix_g'][1], F32, "l1_mix_norm")
    x4 = _pool_fwd(xn1, full['pool_w'], full['pool_scale'], x3, "l1_pool")
    x5, sv_xa1 = _xa_fwd(x4, inp['norm_xa_g'][1], mem_n, full['wq1'], full['wkv_t1'], full['wo1'], "l1_xa", riders)
    x6, sv_ffn1 = _ffn_fwd(x5, inp['norm_ffn_g'][1], full['up_t1'], full['ffn_conv'][1], full['down1'], "l1_ffn", riders)
    loss_part, dx6, dg_final = _loss_head(x6, inp['norm_final_g'], inp['loss_target'][0], "loss_head")

    dx5, dconv1, dg_ffn1 = _ffn_bwd(dx6, x5, inp['norm_ffn_g'][1], full['up_t1'], full['ffn_conv'][1], full['down1'],
                                    sv_ffn1, "l1_ffn", 1, riders)
    dx4, dmem1, dg_xa1 = _xa_bwd(dx5, x4, inp['norm_xa_g'][1], mem_n, full['wq1'], full['wkv_t1'], full['wo1'],
                                 sv_xa1, "l1_xa", 1, riders)
    dxn1, dpool_w, dpool_scale = _pool_bwd(xn1, full['pool_w'], full['pool_scale'], dx4, "l1_pool_bwd")
    pool_pieces = jnp.swapaxes(dpool_w.astype(BF16).reshape(len(POOL_WINDOWS), N_DEV, -1, POOL_GROUP), 0, 1)
    riders.grad('pool_w', pool_pieces.reshape(N_DEV, -1, PACK_COLS))
    dx3, dg_mix1 = riders.run("l1_mix_norm_bwd", _rms_bwd, x3, inp['norm_mix_g'][1], dxn1, dx4)
    dx2, dconv0, dg_ffn0 = _ffn_bwd(dx3, x2, inp['norm_ffn_g'][0], full['up_t0'], full['ffn_conv'][0], full['down0'],
                                    sv_ffn0, "l0_ffn", 0, riders)
    dx1, dmem0, dg_xa0 = _xa_bwd(dx2, x1, inp['norm_xa_g'][0], mem_n, full['wq0'], full['wkv_t0'], full['wo0'],
                                 sv_xa0, "l0_xa", 0, riders)
    dxn0, g_mix0 = _hybrid_bwd(dx1, xn0, wts0, p0, sv_mix0, riders)
    grad_x, dg_mix0 = _rms_bwd(x0, inp['norm_mix_g'][0], dxn0, dx1, "l0_mix_norm_bwd")
    _, dg_mem = _rms_bwd(inp['mem'][0], inp['norm_mem_g'], dmem0 + dmem1, None, "mem_norm_bwd")
    assert not riders.groups and not riders.waiting and all(k.startswith("adamw_") for k in riders.deferred), (
        riders.groups, list(riders.waiting), list(riders.deferred))

    db_in, dc_out, da_row, dd = g_mix0['s5']
    dlb_re, dlb_im, dbb_re, dbb_im, dc_re, dc_im = _s5_matrix_grads(db_in, dc_out, da_row)
    dlam_re, dlam_im, dbr, dbi, dlog_dt = disc_vjp((dlb_re, dlb_im, dbb_re, dbb_im))

    rep_grads = {
        'norm_mix_g': jnp.concatenate([dg_mix0, dg_mix1], 0), 'norm_xa_g': jnp.concatenate([dg_xa0, dg_xa1], 0),
        'norm_ffn_g': jnp.concatenate([dg_ffn0, dg_ffn1], 0), 'norm_mem_g': dg_mem.reshape(-1),
        'norm_final_g': dg_final.reshape(-1), 'a_log_a': g_mix0['a_log_a'], 'dt_bias_a': g_mix0['dt_bias_a'],
        'onorm_g_a': g_mix0['onorm_g_a'], 'ssm_lambda_re': dlam_re[None], 'ssm_lambda_im': dlam_im[None],
        'ssm_b_re': dbr[None], 'ssm_b_im': dbi[None], 'ssm_c_re': dc_re[None], 'ssm_c_im': dc_im[None],
        'ssm_d': dd.reshape(1, N_GROUPS, SSM_GROUP), 'ssm_log_dt': dlog_dt[None], 'b_glu_b': g_mix0['b_glu_b']}
    small_grads = {'conv_qkv_a': g_mix0['conv_qkv_a'][None], 'pool_scale': dpool_scale,
                   'ffn_conv': jnp.stack([dconv0, dconv1])}
    return loss_part, grad_x, riders, rep_grads, small_grads


ADAMW_ORDER = ['ffn_w_up', 'ffn_w_down', 'xa_wkv', 'xa_wq', 'xa_wo', 'w_out_ab', 'w_glu_b', 'pool_w', 'w_in_ab']


def _update(inp, loss_part, grad_x, riders, rep_grads, small_grads):
    dev = _device_index()
    misc_local = _pack([rep_grads[n] for n in REP_NAMES] + [small_grads[n] for n in SMALL_NAMES] + [loss_part])
    (misc_all,) = _sequencer_gather([misc_local], "gather_small_grads", GATHER_LAYER0_ID)
    piece = lambda key: riders.reduced[key]
    both = lambda name: jnp.stack([piece(name + '0'), piece(name + '1')])
    swap = lambda a: jnp.swapaxes(a, -1, -2)
    reduced = {'w_in_ab': lambda: piece('w_in_t')[:W_IN_PIECE][None],
               'w_glu_b': lambda: piece('w_glu').reshape(inp['w_glu_b'].shape),
               'w_out_ab': lambda: piece('w_out')[None], 'pool_w': lambda: piece('pool_w').reshape(inp['pool_w'].shape),
               'xa_wq': lambda: both('wq'), 'xa_wkv': lambda: both('wkv_t'), 'xa_wo': lambda: both('wo'),
               'ffn_w_up': lambda: both('up_t'), 'ffn_w_down': lambda: both('down')}
    transposed = ('w_in_ab', 'xa_wkv', 'ffn_w_up')
    grads, upd = {}, {}
    assert sorted(ADAMW_ORDER) == sorted(BIG_NAMES)
    for n in ADAMW_ORDER:
        fix = swap if n in transposed else (lambda a: a)
        g = reduced[n]()
        out = riders.run("adamw_" + n, _adamw, fix(inp[n]), g, fix(inp['m_' + n]), fix(inp['v_' + n]))
        upd[n], grads[n] = tuple(fix(o) for o in out), fix(g)
    assert not riders.waiting and not riders.deferred, (list(riders.waiting), list(riders.deferred))
    misc_sum = _sum_leading(misc_all, "small_grads_sum")
    misc = _unpack(misc_sum, [inp[n].shape for n in REP_NAMES] + [SMALL_SHARDED[n][0] for n in SMALL_NAMES] + [()])
    loss = misc.pop()
    for n, g in zip(REP_NAMES, misc):
        grads[n] = g
    for n, g in zip(SMALL_NAMES, misc[len(REP_NAMES):]):
        grads[n] = lax.dynamic_index_in_dim(_split_shards(g, SMALL_SHARDED[n][1]), dev, 0, keepdims=False
                                            ).reshape(inp[n].shape)
    tiny_names = REP_NAMES + SMALL_NAMES
    rep_total = sum(int(np.prod(inp[n].shape)) for n in REP_NAMES)
    packs = [_pack([inp[prefix + n] for n in tiny_names]) for prefix in ('', 'm_', 'v_')]
    g_pack = _pack([misc_sum.reshape(-1)[:rep_total]] + [grads[n] for n in SMALL_NAMES])
    tiny_out = [_unpack(o, [inp[n].shape for n in tiny_names])
                for o in _adamw(packs[0], g_pack, packs[1], packs[2], "adamw_small")]
    for i, n in enumerate(tiny_names):
        upd[n] = tuple(o[i] for o in tiny_out)

    outs = [loss, grad_x[None]]
    outs += [grads[n] for n in WEIGHT_NAMES]
    for i in range(3):
        outs += [upd[n][i] for n in WEIGHT_NAMES]
    return tuple(outs)


def _step(inp):
    loss_part, grad_x, riders, rep_grads, small_grads = _local_step(inp)
    return _update(inp, loss_part, grad_x, riders, rep_grads, small_grads)


INPUT_NAMES = (['x', 'mem'] + WEIGHT_NAMES + ['loss_target'] + ['m_' + n for n in WEIGHT_NAMES]
               + ['v_' + n for n in WEIGHT_NAMES])


def kernel(x, mem, norm_mix_g, norm_xa_g, norm_ffn_g, norm_mem_g, norm_final_g, w_in_ab, conv_qkv_a, a_log_a, dt_bias_a, onorm_g_a, ssm_lambda_re, ssm_lambda_im, ssm_b_re, ssm_b_im, ssm_c_re, ssm_c_im, ssm_d, ssm_log_dt, w_glu_b, b_glu_b, w_out_ab, pool_w, pool_scale, xa_wq, xa_wkv, xa_wo, ffn_w_up, ffn_conv, ffn_w_down, loss_target, m_norm_mix_g, m_norm_xa_g, m_norm_ffn_g, m_norm_mem_g, m_norm_final_g, m_w_in_ab, m_conv_qkv_a, m_a_log_a, m_dt_bias_a, m_onorm_g_a, m_ssm_lambda_re, m_ssm_lambda_im, m_ssm_b_re, m_ssm_b_im, m_ssm_c_re, m_ssm_c_im, m_ssm_d, m_ssm_log_dt, m_w_glu_b, m_b_glu_b, m_w_out_ab, m_pool_w, m_pool_scale, m_xa_wq, m_xa_wkv, m_xa_wo, m_ffn_w_up, m_ffn_conv, m_ffn_w_down, v_norm_mix_g, v_norm_xa_g, v_norm_ffn_g, v_norm_mem_g, v_norm_final_g, v_w_in_ab, v_conv_qkv_a, v_a_log_a, v_dt_bias_a, v_onorm_g_a, v_ssm_lambda_re, v_ssm_lambda_im, v_ssm_b_re, v_ssm_b_im, v_ssm_c_re, v_ssm_c_im, v_ssm_d, v_ssm_log_dt, v_w_glu_b, v_b_glu_b, v_w_out_ab, v_pool_w, v_pool_scale, v_xa_wq, v_xa_wkv, v_xa_wo, v_ffn_w_up, v_ffn_conv, v_ffn_w_down):
    args = (x, mem, norm_mix_g, norm_xa_g, norm_ffn_g, norm_mem_g, norm_final_g, w_in_ab, conv_qkv_a, a_log_a, dt_bias_a, onorm_g_a, ssm_lambda_re, ssm_lambda_im, ssm_b_re, ssm_b_im, ssm_c_re, ssm_c_im, ssm_d, ssm_log_dt, w_glu_b, b_glu_b, w_out_ab, pool_w, pool_scale, xa_wq, xa_wkv, xa_wo, ffn_w_up, ffn_conv, ffn_w_down, loss_target, m_norm_mix_g, m_norm_xa_g, m_norm_ffn_g, m_norm_mem_g, m_norm_final_g, m_w_in_ab, m_conv_qkv_a, m_a_log_a, m_dt_bias_a, m_onorm_g_a, m_ssm_lambda_re, m_ssm_lambda_im, m_ssm_b_re, m_ssm_b_im, m_ssm_c_re, m_ssm_c_im, m_ssm_d, m_ssm_log_dt, m_w_glu_b, m_b_glu_b, m_w_out_ab, m_pool_w, m_pool_scale, m_xa_wq, m_xa_wkv, m_xa_wo, m_ffn_w_up, m_ffn_conv, m_ffn_w_down, v_norm_mix_g, v_norm_xa_g, v_norm_ffn_g, v_norm_mem_g, v_norm_final_g, v_w_in_ab, v_conv_qkv_a, v_a_log_a, v_dt_bias_a, v_onorm_g_a, v_ssm_lambda_re, v_ssm_lambda_im, v_ssm_b_re, v_ssm_b_im, v_ssm_c_re, v_ssm_c_im, v_ssm_d, v_ssm_log_dt, v_w_glu_b, v_b_glu_b, v_w_out_ab, v_pool_w, v_pool_scale, v_xa_wq, v_xa_wkv, v_xa_wo, v_ffn_w_up, v_ffn_conv, v_ffn_w_down)
    return _step(dict(zip(INPUT_NAMES, args)))
```

```python
import functools
import math

import numpy as np
import jax
import jax.numpy as jnp
from jax import lax
from jax.experimental import pallas as pl
from jax.experimental.pallas import tpu as pltpu
from jax.experimental.pallas import tpu_sc as plsc

F32, BF16 = jnp.float32, jnp.bfloat16
HIGH, HIGHEST = lax.Precision.HIGH, lax.Precision.HIGHEST
MESH = pl.DeviceIdType.MESH

N_DEV = 8
SEQ, D_MODEL, MEM_LEN = 2048, 1024, 256
WIDTH_A, N_HEADS_A, HEAD_A, CONV_A = 512, 4, 128, 4
GDR_CHUNK = 128
SSM_WIDTH, SSM_GROUP, N_GROUPS, SSM_STATE = 512, 16, 32, 64
SSM_CH = N_GROUPS * SSM_STATE
SCAN_CB = 512
POOL_WINDOWS = (2, 4, 8, 16)
POOL_GROUP = 256
N_HEADS_X, HEAD_X = 4, 256
D_FF, CONV_FFN = 2816, 3
RMS_EPS = 1e-6
ADAM_LR, ADAM_B1, ADAM_B2, ADAM_EPS, ADAM_WD, ADAM_STEP = 0.001, 0.9, 0.999, 1e-08, 0.01, 10
LANE = 128
PACK_COLS = 1024
VMEM_LIMIT_BYTES = 56 * 1024 * 1024


def _params(sem=None):
    return pltpu.CompilerParams(dimension_semantics=sem, vmem_limit_bytes=VMEM_LIMIT_BYTES)


class Comm:
    def __init__(self, inputs, out_shapes, sems, start, end, mid=None):
        self.inputs, self.out_shapes, self.sems = list(inputs), list(out_shapes), list(sems)
        self.start, self.mid, self.end = start, mid, end


def _merge_comms(comms):
    comms = [c for c in comms if c is not None]
    if not comms:
        return None, []
    bounds, ni, no, ns = [], 0, 0, 0
    for c in comms:
        bounds.append((ni, no, ns))
        ni, no, ns = ni + len(c.inputs), no + len(c.out_shapes), ns + len(c.sems)

    def phase(which):
        def run(ins, outs, sems):
            for c, (i0, o0, s0) in zip(comms, bounds):
                fn = getattr(c, which)
                if fn is not None:
                    fn(ins[i0:i0 + len(c.inputs)], outs[o0:o0 + len(c.out_shapes)], sems[s0:s0 + len(c.sems)])
        return run

    merged = Comm([a for c in comms for a in c.inputs], [s for c in comms for s in c.out_shapes],
                  [s for c in comms for s in c.sems], phase("start"), phase("end"), phase("mid"))
    return merged, [(o0, o0 + len(c.out_shapes)) for c, (_, o0, _) in zip(comms, bounds)]


def _call(body, *, name, grid, in_specs, out_specs, out_shape, args, scratch_shapes=(), sem=None, comm=None):
    single = not isinstance(out_shape, (list, tuple))
    out_specs_l = [out_specs] if single else list(out_specs)
    out_shape_l = [out_shape] if single else list(out_shape)
    scratch_shapes = list(scratch_shapes)
    merged, spans = _merge_comms(comm if isinstance(comm, (list, tuple)) else [comm])
    if merged is None:
        outs = pl.pallas_call(body, name=name, grid=grid, in_specs=list(in_specs), out_specs=out_specs_l,
                              out_shape=out_shape_l, scratch_shapes=scratch_shapes, compiler_params=_params(sem))(*args)
        outs = outs[0] if single else outs
        return outs if comm is None else (outs, [])
    n_in, n_out, n_scr = len(in_specs), len(out_specs_l), len(scratch_shapes)
    ci, co = len(merged.inputs), len(merged.out_shapes)
    total = int(np.prod(grid))

    def wrapped(*refs):
        ins, cins = refs[:n_in], refs[n_in:n_in + ci]
        outs, couts = refs[n_in + ci:n_in + ci + n_out], refs[n_in + ci + n_out:n_in + ci + n_out + co]
        scr, csems = refs[n_in + ci + n_out + co:n_in + ci + n_out + co + n_scr], refs[n_in + ci + n_out + co + n_scr:]
        lin = pl.program_id(0)
        for d in range(1, len(grid)):
            lin = lin * grid[d] + pl.program_id(d)
        pl.when(lin == 0)(lambda: merged.start(cins, couts, csems))
        body(*ins, *outs, *scr)
        mid_step = min((3 * total) // 4, total - 1)
        pl.when(lin == mid_step)(lambda: merged.mid(cins, couts, csems))
        pl.when(lin == total - 1)(lambda: merged.end(cins, couts, csems))

    any_spec = pl.BlockSpec(memory_space=pl.ANY)
    res = pl.pallas_call(
        wrapped, name=name, grid=grid, in_specs=list(in_specs) + [any_spec] * ci,
        out_specs=out_specs_l + [any_spec] * co, out_shape=out_shape_l + merged.out_shapes,
        scratch_shapes=scratch_shapes + merged.sems,
        compiler_params=_params(("arbitrary",) * len(grid)))(*args, *merged.inputs)
    outs, couts = res[:n_out], res[n_out:]
    return (outs[0] if single else list(outs)), [list(couts[a:b]) for a, b in spans]


def _comm_only(comm, name):
    def body():
        pass

    _, couts = _call(body, name=name, grid=(1,), in_specs=[], out_specs=[], out_shape=[], args=[], comm=comm)
    return couts[0]


def _tile(dim, pref):
    best = None
    for t in range(LANE, min(dim, pref) + 1, LANE):
        if dim % t == 0:
            best = t
    return best if best is not None else dim


MM_VMEM_BUDGET = 40 * 1024 * 1024


def _mm_tiles(m, n, k, a_bytes, b_bytes, o_bytes, r_bytes):
    for tk in (k, _tile(k, 2048), _tile(k, 1024), _tile(k, 512)):
        for tm, tn in ((1024, 1536), (1024, 1024), (1024, 512), (512, 512), (256, 512), (256, 256)):
            tm, tn = _tile(m, tm), _tile(n, tn)
            acc = 0 if tk == k else tm * tn * 4
            need = 2 * (tm * tk * a_bytes + tk * tn * b_bytes + tm * tn * (o_bytes + r_bytes)) + acc
            if need <= MM_VMEM_BUDGET:
                return tm, tn, tk
    raise ValueError("no matmul tiling fits VMEM")


def _mm(a, b, mode, name, out_dtype=F32, res=None, comm=None):
    if mode == "nn":
        (m, k), n = a.shape, b.shape[1]
    elif mode == "nt":
        (m, k), n = a.shape, b.shape[0]
    else:
        (k, m), n = a.shape, b.shape[1]
    tm, tn, tk = _mm_tiles(m, n, k, a.dtype.itemsize, b.dtype.itemsize, jnp.dtype(out_dtype).itemsize,
                           0 if res is None else res.dtype.itemsize)
    nk = k // tk
    dims = {"nn": ((1,), (0,)), "nt": ((1,), (1,)), "tn": ((0,), (0,))}[mode]

    def body(*refs):
        if res is None:
            a_ref, b_ref, o_ref = refs[:3]
            r_ref = None
        else:
            a_ref, b_ref, r_ref, o_ref = refs[:4]
        part = lax.dot_general(a_ref[...].astype(BF16), b_ref[...].astype(BF16), (dims, ((), ())),
                               preferred_element_type=F32)

        def finish(out):
            if r_ref is not None:
                out = out + r_ref[...].astype(F32)
            o_ref[...] = out.astype(out_dtype)

        if nk == 1:
            finish(part)
            return
        acc = refs[-1]
        kk = pl.program_id(2)

        @pl.when(kk == 0)
        def _():
            acc[...] = part

        @pl.when(kk > 0)
        def _():
            acc[...] += part

        @pl.when(kk == nk - 1)
        def _():
            finish(acc[...])

    a_spec = (pl.BlockSpec((tk, tm), lambda i, j, q: (q, i)) if mode == "tn"
              else pl.BlockSpec((tm, tk), lambda i, j, q: (i, q)))
    b_spec = (pl.BlockSpec((tn, tk), lambda i, j, q: (j, q)) if mode == "nt"
              else pl.BlockSpec((tk, tn), lambda i, j, q: (q, j)))
    o_spec = pl.BlockSpec((tm, tn), lambda i, j, q: (i, j))
    in_specs, args = [a_spec, b_spec], [a, b]
    if res is not None:
        in_specs.append(o_spec)
        args.append(res)
    return _call(body, name=name, grid=(m // tm, n // tn, nk), in_specs=in_specs, out_specs=o_spec,
                 out_shape=jax.ShapeDtypeStruct((m, n), out_dtype),
                 scratch_shapes=[] if nk == 1 else [pltpu.VMEM((tm, tn), F32)],
                 sem=("parallel", "parallel", "arbitrary"), args=args, comm=comm)


def _mm_parts(parts, b, mode, name, out_dtype=F32):
    count = len(parts)
    rows, cols = parts[0].shape
    n = b.shape[1]
    if mode == "nn":
        tm, tn, tk = _mm_tiles(rows, n, count * cols, parts[0].dtype.itemsize, b.dtype.itemsize,
                               jnp.dtype(out_dtype).itemsize, 0)
        assert tk == count * cols

        def body(*refs):
            a_refs, b_refs, o_ref = refs[:count], refs[count:2 * count], refs[2 * count]
            out = None
            for a_ref, b_ref in zip(a_refs, b_refs):
                part = jnp.dot(a_ref[...].astype(BF16), b_ref[...].astype(BF16), preferred_element_type=F32)
                out = part if out is None else out + part
            o_ref[...] = out.astype(out_dtype)

        return _call(body, name=name, grid=(rows // tm, n // tn),
                     in_specs=[pl.BlockSpec((tm, cols), lambda i, j: (i, 0))] * count
                     + [pl.BlockSpec((cols, tn), functools.partial(lambda q, i, j: (q, j), q)) for q in range(count)],
                     out_specs=pl.BlockSpec((tm, tn), lambda i, j: (i, j)),
                     out_shape=jax.ShapeDtypeStruct((rows, n), out_dtype),
                     sem=("parallel", "parallel"), args=(*parts, *([b] * count)))
    tm, tn = _tile(cols, 1536), _tile(n, 1024)
    per = cols // tm

    def body_t(*refs):
        a_refs, b_ref, o_ref = refs[:count], refs[count], refs[count + 1]
        for q, a_ref in enumerate(a_refs):
            @pl.when(pl.program_id(0) // per == q)
            def _(a_ref=a_ref):
                o_ref[...] = lax.dot_general(a_ref[...].astype(BF16), b_ref[...].astype(BF16),
                                             (((0,), (0,)), ((), ())), preferred_element_type=F32).astype(out_dtype)

    return _call(body_t, name=name, grid=(count * per, n // tn),
                 in_specs=[pl.BlockSpec((rows, tm), functools.partial(lambda q, i, j: (0, jnp.clip(i - q * per, 0, per - 1)), q))
                           for q in range(count)] + [pl.BlockSpec((rows, tn), lambda i, j: (0, j))],
                 out_specs=pl.BlockSpec((tm, tn), lambda i, j: (i, j)),
                 out_shape=jax.ShapeDtypeStruct((count * cols, n), out_dtype),
                 sem=("parallel", "parallel"), args=(*parts, b))


def _mm_bd(a, b, mode, name, out_dtype=F32, res=None, comm=None, tm=1024):
    if mode == "tn":
        k = a.shape[0]
        nb = min(a.shape[1], b.shape[1]) // LANE
        ma, n = a.shape[1] // nb, b.shape[1] // nb

        def body(a_ref, b_ref, o_ref):
            o_ref[0] = lax.dot_general(a_ref[...].astype(BF16), b_ref[...].astype(BF16), (((0,), (0,)), ((), ())),
                                       preferred_element_type=F32).astype(out_dtype)

        return _call(body, name=name, grid=(nb,),
                     in_specs=[pl.BlockSpec((k, ma), lambda j: (0, j)), pl.BlockSpec((k, n), lambda j: (0, j))],
                     out_specs=pl.BlockSpec((1, ma, n), lambda j: (j, 0, 0)),
                     out_shape=jax.ShapeDtypeStruct((nb, ma, n), out_dtype), sem=("parallel",), args=(a, b), comm=comm)
    m = a.shape[0]
    nb = b.shape[0]
    ka = a.shape[1] // nb
    n = b.shape[2] if mode == "nn" else b.shape[1]
    tm = _tile(m, tm)
    dims = ((1,), (0,)) if mode == "nn" else ((1,), (1,))

    def body(*refs):
        if res is None:
            a_ref, b_ref, o_ref = refs
            r_ref = None
        else:
            a_ref, b_ref, r_ref, o_ref = refs
        out = lax.dot_general(a_ref[...].astype(BF16), b_ref[0].astype(BF16), (dims, ((), ())),
                              preferred_element_type=F32)
        if r_ref is not None:
            out = out + r_ref[...].astype(F32)
        o_ref[...] = out.astype(out_dtype)

    o_spec = pl.BlockSpec((tm, n), lambda i, j: (i, j))
    in_specs = [pl.BlockSpec((tm, ka), lambda i, j: (i, j)), pl.BlockSpec((1,) + b.shape[1:], lambda i, j: (j, 0, 0))]
    args = [a, b]
    if res is not None:
        in_specs.append(o_spec)
        args.append(res)
    return _call(body, name=name, grid=(m // tm, nb), in_specs=in_specs, out_specs=o_spec,
                 out_shape=jax.ShapeDtypeStruct((m, nb * n), out_dtype), sem=("parallel", "parallel"),
                 args=args, comm=comm)


def _rms_fwd(x, g, out_dtype, name, tr=256):
    rows, d = x.shape

    def body(x_ref, g_ref, o_ref):
        xv = x_ref[...]
        r = lax.rsqrt(jnp.mean(xv * xv, axis=-1, keepdims=True) + RMS_EPS)
        o_ref[...] = (xv * r * g_ref[...]).astype(out_dtype)

    return pl.pallas_call(
        body, name=name, grid=(rows // tr,),
        in_specs=[pl.BlockSpec((tr, d), lambda i: (i, 0)), pl.BlockSpec((1, d), lambda i: (0, 0))],
        out_specs=pl.BlockSpec((tr, d), lambda i: (i, 0)), out_shape=jax.ShapeDtypeStruct((rows, d), out_dtype),
        compiler_params=_params(("parallel",)))(x, g.reshape(1, d))


def _rms_bwd(x, g, dy, dres, name, tr=256, comm=None):
    rows, d = x.shape

    def body(*refs):
        if dres is None:
            x_ref, g_ref, dy_ref, dx_ref, dg_ref = refs
            r_ref = None
        else:
            x_ref, g_ref, dy_ref, r_ref, dx_ref, dg_ref = refs

        @pl.when(pl.program_id(0) == 0)
        def _():
            dg_ref[...] = jnp.zeros_like(dg_ref)

        xv, dyv = x_ref[...], dy_ref[...].astype(F32)
        r = lax.rsqrt(jnp.mean(xv * xv, axis=-1, keepdims=True) + RMS_EPS)
        xh = xv * r
        dyg = dyv * g_ref[...]
        dx = r * (dyg - xh * jnp.mean(dyg * xh, axis=-1, keepdims=True))
        if r_ref is not None:
            dx = dx + r_ref[...]
        dx_ref[...] = dx
        dg_ref[...] += jnp.sum(dyv * xh, axis=0, keepdims=True)

    blk = pl.BlockSpec((tr, d), lambda i: (i, 0))
    vec = pl.BlockSpec((1, d), lambda i: (0, 0))
    in_specs, args = [blk, vec, blk], [x, g.reshape(1, d), dy]
    if dres is not None:
        in_specs.append(blk)
        args.append(dres)
    return _call(
        body, name=name, grid=(rows // tr,), in_specs=in_specs, out_specs=[blk, vec],
        out_shape=[jax.ShapeDtypeStruct((rows, d), F32), jax.ShapeDtypeStruct((1, d), F32)],
        sem=("arbitrary",), args=args, comm=comm)


def _loss_head(x, g, target, name, tr=256):
    rows, d = x.shape

    def body(x_ref, g_ref, t_ref, loss_ref, dx_ref, dg_ref):
        @pl.when(pl.program_id(0) == 0)
        def _():
            dg_ref[...] = jnp.zeros_like(dg_ref)
            loss_ref[...] = jnp.zeros_like(loss_ref)

        xv = x_ref[...]
        r = lax.rsqrt(jnp.mean(xv * xv, axis=-1, keepdims=True) + RMS_EPS)
        xh = xv * r
        err = xh * g_ref[...] - t_ref[...]
        loss_ref[...] += 0.5 * jnp.sum(jnp.mean(err * err, axis=-1, keepdims=True), keepdims=True)
        dyv = err * (1.0 / d)
        dyg = dyv * g_ref[...]
        dx_ref[...] = r * (dyg - xh * jnp.mean(dyg * xh, axis=-1, keepdims=True))
        dg_ref[...] += jnp.sum(dyv * xh, axis=0, keepdims=True)

    blk = pl.BlockSpec((tr, d), lambda i: (i, 0))
    vec = pl.BlockSpec((1, d), lambda i: (0, 0))
    return pl.pallas_call(
        body, name=name, grid=(rows // tr,), in_specs=[blk, vec, blk],
        out_specs=[pl.BlockSpec((1, 1), lambda i: (0, 0)), blk, vec],
        out_shape=[jax.ShapeDtypeStruct((1, 1), F32), jax.ShapeDtypeStruct((rows, d), F32),
                   jax.ShapeDtypeStruct((1, d), F32)],
        compiler_params=_params(("arbitrary",)))(x, g.reshape(1, d), target)


def _shift_down(x, s):
    rows = lax.broadcasted_iota(jnp.int32, x.shape, 0)
    return jnp.where(rows >= s, pltpu.roll(x, s, 0), 0.0)


def _shift_up(x, s):
    n = x.shape[0]
    rows = lax.broadcasted_iota(jnp.int32, x.shape, 0)
    return jnp.where(rows < n - s, pltpu.roll(x, n - s, 0), 0.0)


def _sigmoid(x):
    return 1.0 / (1.0 + jnp.exp(-x))


def _silu_and_grad(x):
    s = _sigmoid(x)
    return x * s, s * (1.0 + x * (1.0 - s))


_GELU_C0, _GELU_C1 = math.sqrt(2.0 / math.pi), 0.044715


def _gelu_and_grad(x):
    th = jnp.tanh(_GELU_C0 * (x + _GELU_C1 * x * x * x))
    y = 0.5 * x * (1.0 + th)
    dy = 0.5 * (1.0 + th) + 0.5 * x * (1.0 - th * th) * _GELU_C0 * (1.0 + 3.0 * _GELU_C1 * x * x)
    return y, dy


def _ffn_act_fwd(h, w, name, tc=256, comm=None):
    t = h.shape[0]
    nb = D_FF // tc

    def body(hg_ref, hv_ref, wg_ref, wv_ref, a_ref):
        def conv(x, wr):
            return wr[2:3, :] * x + wr[1:2, :] * _shift_down(x, 1) + wr[0:1, :] * _shift_down(x, 2)

        cg = conv(hg_ref[...], wg_ref[...])
        cv = conv(hv_ref[...], wv_ref[...])
        a_ref[...] = (cg * _sigmoid(cg) * cv).astype(BF16)

    return _call(
        body, name=name, grid=(nb,),
        in_specs=[pl.BlockSpec((t, tc), lambda j: (0, j)), pl.BlockSpec((t, tc), lambda j: (0, j + nb)),
                  pl.BlockSpec((CONV_FFN, tc), lambda j: (0, j)), pl.BlockSpec((CONV_FFN, tc), lambda j: (0, j + nb))],
        out_specs=pl.BlockSpec((t, tc), lambda j: (0, j)), out_shape=jax.ShapeDtypeStruct((t, D_FF), BF16),
        sem=("parallel",), args=(h, h, w, w), comm=comm)


def _ffn_act_bwd(h, w, da, name, tc=256, comm=None):
    t = h.shape[0]
    nb = D_FF // tc

    def body(hg_ref, hv_ref, wg_ref, wv_ref, da_ref, dhg_ref, dhv_ref, dwg_ref, dwv_ref):
        hg, hv, wg, wv = hg_ref[...], hv_ref[...], wg_ref[...], wv_ref[...]
        hg1, hg2, hv1, hv2 = _shift_down(hg, 1), _shift_down(hg, 2), _shift_down(hv, 1), _shift_down(hv, 2)
        cg = wg[2:3, :] * hg + wg[1:2, :] * hg1 + wg[0:1, :] * hg2
        cv = wv[2:3, :] * hv + wv[1:2, :] * hv1 + wv[0:1, :] * hv2
        sg, dsg = _silu_and_grad(cg)
        dav = da_ref[...].astype(F32)
        dcv = dav * sg
        dcg = dav * cv * dsg

        def conv_t(dc, wr):
            return wr[2:3, :] * dc + wr[1:2, :] * _shift_up(dc, 1) + wr[0:1, :] * _shift_up(dc, 2)

        dhg_ref[...] = conv_t(dcg, wg).astype(BF16)
        dhv_ref[...] = conv_t(dcv, wv).astype(BF16)
        dwg_ref[0:1, :] = jnp.sum(dcg * hg2, axis=0, keepdims=True)
        dwg_ref[1:2, :] = jnp.sum(dcg * hg1, axis=0, keepdims=True)
        dwg_ref[2:3, :] = jnp.sum(dcg * hg, axis=0, keepdims=True)
        dwv_ref[0:1, :] = jnp.sum(dcv * hv2, axis=0, keepdims=True)
        dwv_ref[1:2, :] = jnp.sum(dcv * hv1, axis=0, keepdims=True)
        dwv_ref[2:3, :] = jnp.sum(dcv * hv, axis=0, keepdims=True)

    big = lambda off: pl.BlockSpec((t, tc), lambda j: (0, j + off))
    small = lambda off: pl.BlockSpec((CONV_FFN, tc), lambda j: (0, j + off))
    res = _call(
        body, name=name, grid=(nb,),
        in_specs=[big(0), big(nb), small(0), small(nb), big(0)],
        out_specs=[big(0), big(0), small(0), small(0)],
        out_shape=[jax.ShapeDtypeStruct((t, D_FF), BF16), jax.ShapeDtypeStruct((t, D_FF), BF16),
                   jax.ShapeDtypeStruct((CONV_FFN, D_FF), F32), jax.ShapeDtypeStruct((CONV_FFN, D_FF), F32)],
        sem=("parallel",), args=(h, h, w, w, da), comm=comm)
    (dhg, dhv, dwg, dwv), couts = res if comm is not None else (res, None)
    out = ((dhg, dhv), jnp.concatenate([dwg, dwv], axis=1))
    return out if comm is None else (out, couts)


def _attn_probs(q, k):
    s = lax.dot_general(q.astype(BF16), k.astype(BF16), (((1,), (1,)), ((), ())),
                        preferred_element_type=F32) * (HEAD_X ** -0.5)
    s = s - jnp.max(s, axis=-1, keepdims=True)
    p = jnp.exp(s)
    return p / jnp.sum(p, axis=-1, keepdims=True)


def _attn_fwd(q, kv, name, tq=512, comm=None):
    t = q.shape[0]

    def body(q_ref, k_ref, v_ref, o_ref):
        p = _attn_probs(q_ref[...], k_ref[...])
        o_ref[...] = jnp.dot(p.astype(BF16), v_ref[...].astype(BF16), preferred_element_type=F32).astype(BF16)

    return _call(
        body, name=name, grid=(N_HEADS_X, t // tq),
        in_specs=[pl.BlockSpec((tq, HEAD_X), lambda h, i: (i, h)),
                  pl.BlockSpec((MEM_LEN, HEAD_X), lambda h, i: (0, h)),
                  pl.BlockSpec((MEM_LEN, HEAD_X), lambda h, i: (0, h + N_HEADS_X))],
        out_specs=pl.BlockSpec((tq, HEAD_X), lambda h, i: (i, h)),
        out_shape=jax.ShapeDtypeStruct((t, N_HEADS_X * HEAD_X), BF16),
        sem=("parallel", "parallel"), args=(q, kv, kv), comm=comm)


def _attn_bwd(q, kv, do, name, tq=512):
    t = q.shape[0]

    def body(q_ref, k_ref, v_ref, do_ref, dq_ref, dk_ref, dv_ref):
        @pl.when(pl.program_id(1) == 0)
        def _():
            dk_ref[...] = jnp.zeros_like(dk_ref)
            dv_ref[...] = jnp.zeros_like(dv_ref)

        qb, kb, vb, dob = (r[...].astype(BF16) for r in (q_ref, k_ref, v_ref, do_ref))
        p = _attn_probs(qb, kb)
        dp = lax.dot_general(dob, vb, (((1,), (1,)), ((), ())), preferred_element_type=F32)
        ds = p * (dp - jnp.sum(dp * p, axis=-1, keepdims=True)) * (HEAD_X ** -0.5)
        dsb = ds.astype(BF16)
        dq_ref[...] = jnp.dot(dsb, kb, preferred_element_type=F32).astype(BF16)
        dk_ref[...] += lax.dot_general(dsb, qb, (((0,), (0,)), ((), ())), preferred_element_type=F32)
        dv_ref[...] += lax.dot_general(p.astype(BF16), dob, (((0,), (0,)), ((), ())), preferred_element_type=F32)

    qs = pl.BlockSpec((tq, HEAD_X), lambda h, i: (i, h))
    ms = pl.BlockSpec((MEM_LEN, HEAD_X), lambda h, i: (0, h))
    return pl.pallas_call(
        body, name=name, grid=(N_HEADS_X, t // tq),
        in_specs=[qs, ms, pl.BlockSpec((MEM_LEN, HEAD_X), lambda h, i: (0, h + N_HEADS_X)), qs],
        out_specs=[qs, ms, ms],
        out_shape=[jax.ShapeDtypeStruct((t, D_MODEL), BF16), jax.ShapeDtypeStruct((MEM_LEN, D_MODEL), F32),
                   jax.ShapeDtypeStruct((MEM_LEN, D_MODEL), F32)],
        compiler_params=_params(("parallel", "arbitrary")))(q, kv, kv, do)


def _pool_counts(t, win):
    pos = lax.broadcasted_iota(jnp.int32, (t, 1), 0).astype(F32) + 1.0
    return 1.0 / jnp.minimum(pos, float(win))


def _pool_delta(xv, win):
    s, step = xv, 1
    while step < win:
        s = s + _shift_down(s, step)
        step *= 2
    return s * _pool_counts(xv.shape[0], win) - xv


def _pool_delta_t(dv, win):
    s, step = dv * _pool_counts(dv.shape[0], win), 1
    while step < win:
        s = s + _shift_up(s, step)
        step *= 2
    return s - dv


def _pool_fwd(xn, w, scale, res, name):
    t = xn.shape[0]

    def make_branch(win, xn_ref, w_ref, s_ref, r_ref, o_ref):
        def branch():
            dl = _pool_delta(xn_ref[...], win)
            y = jnp.dot(dl.astype(BF16), w_ref[0], preferred_element_type=F32)
            o_ref[...] = r_ref[...] + y * s_ref[...]
        return branch

    def body(xn_ref, w_ref, s_ref, r_ref, o_ref):
        for gi, win in enumerate(POOL_WINDOWS):
            pl.when(pl.program_id(0) == gi)(make_branch(win, xn_ref, w_ref, s_ref, r_ref, o_ref))

    blk = pl.BlockSpec((t, POOL_GROUP), lambda g: (0, g))
    return pl.pallas_call(
        body, name=name, grid=(len(POOL_WINDOWS),),
        in_specs=[blk, pl.BlockSpec((1, POOL_GROUP, POOL_GROUP), lambda g: (g, 0, 0)),
                  pl.BlockSpec((1, POOL_GROUP), lambda g: (0, g)), blk],
        out_specs=blk, out_shape=jax.ShapeDtypeStruct((t, D_MODEL), F32),
        compiler_params=_params(("parallel",)))(xn, w, scale, res)


def _pool_bwd(xn, w, scale, dmix, name):
    t = xn.shape[0]

    def make_branch(win, xn_ref, w_ref, s_ref, d_ref, dxn_ref, dw_ref, ds_ref):
        def branch():
            dl = _pool_delta(xn_ref[...], win).astype(BF16)
            wv = w_ref[0]
            dm = d_ref[...]
            y = jnp.dot(dl, wv, preferred_element_type=F32)
            ds_ref[...] = jnp.sum(dm * y, axis=0, keepdims=True)
            dy = (dm * s_ref[...]).astype(BF16)
            dw_ref[0] = lax.dot_general(dl, dy, (((0,), (0,)), ((), ())), preferred_element_type=F32)
            ddl = lax.dot_general(dy, wv, (((1,), (1,)), ((), ())), preferred_element_type=F32)
            dxn_ref[...] = _pool_delta_t(ddl, win)
        return branch

    def body(*refs):
        for gi, win in enumerate(POOL_WINDOWS):
            pl.when(pl.program_id(0) == gi)(make_branch(win, *refs))

    blk = pl.BlockSpec((t, POOL_GROUP), lambda g: (0, g))
    wspec = pl.BlockSpec((1, POOL_GROUP, POOL_GROUP), lambda g: (g, 0, 0))
    vec = pl.BlockSpec((1, POOL_GROUP), lambda g: (0, g))
    return pl.pallas_call(
        body, name=name, grid=(len(POOL_WINDOWS),), in_specs=[blk, wspec, vec, blk], out_specs=[blk, wspec, vec],
        out_shape=[jax.ShapeDtypeStruct((t, D_MODEL), F32),
                   jax.ShapeDtypeStruct((len(POOL_WINDOWS), POOL_GROUP, POOL_GROUP), F32),
                   jax.ShapeDtypeStruct((1, D_MODEL), F32)],
        compiler_params=_params(("parallel",)))(xn, w, scale, dmix)


def _qkv_conv(h, wr):
    return (wr[3:4, :] * h + wr[2:3, :] * _shift_down(h, 1) + wr[1:2, :] * _shift_down(h, 2)
            + wr[0:1, :] * _shift_down(h, 3))


def _qkv_block_kind(j):
    return j < 2 * N_HEADS_A, jnp.where(j < N_HEADS_A, HEAD_A ** -0.5, 1.0)


def _qkv_pre_fwd(h, w, name):
    t, cols = h.shape

    def body(h_ref, w_ref, o_ref):
        normalised, scale = _qkv_block_kind(pl.program_id(0))
        c = _qkv_conv(h_ref[...], w_ref[...])
        s = c * _sigmoid(c)
        r = lax.rsqrt(jnp.sum(s * s, axis=-1, keepdims=True) + 1e-6)
        o_ref[...] = jnp.where(normalised, s * (r * scale), s)

    blk = pl.BlockSpec((t, HEAD_A), lambda j: (0, j))
    return pl.pallas_call(
        body, name=name, grid=(cols // HEAD_A,), in_specs=[blk, pl.BlockSpec((CONV_A, HEAD_A), lambda j: (0, j))],
        out_specs=blk, out_shape=jax.ShapeDtypeStruct((t, cols), F32), compiler_params=_params(("parallel",)))(h, w)


def _qkv_pre_bwd(h, w, dy, name):
    t, cols = h.shape

    def body(h_ref, w_ref, dy_ref, dh_ref, dw_ref):
        normalised, scale = _qkv_block_kind(pl.program_id(0))
        hv, wr, dyv = h_ref[...], w_ref[...], dy_ref[...]
        h1, h2, h3 = _shift_down(hv, 1), _shift_down(hv, 2), _shift_down(hv, 3)
        c = wr[3:4, :] * hv + wr[2:3, :] * h1 + wr[1:2, :] * h2 + wr[0:1, :] * h3
        s, dsilu = _silu_and_grad(c)
        r = lax.rsqrt(jnp.sum(s * s, axis=-1, keepdims=True) + 1e-6)
        y = s * r
        dys = dyv * scale
        ds = jnp.where(normalised, r * (dys - y * jnp.sum(dys * y, axis=-1, keepdims=True)), dyv)
        dc = ds * dsilu
        dh = (wr[3:4, :] * dc + wr[2:3, :] * _shift_up(dc, 1) + wr[1:2, :] * _shift_up(dc, 2)
              + wr[0:1, :] * _shift_up(dc, 3))
        dh_ref[...] = dh.astype(BF16)
        dw_ref[0:1, :] = jnp.sum(dc * h3, axis=0, keepdims=True)
        dw_ref[1:2, :] = jnp.sum(dc * h2, axis=0, keepdims=True)
        dw_ref[2:3, :] = jnp.sum(dc * h1, axis=0, keepdims=True)
        dw_ref[3:4, :] = jnp.sum(dc * hv, axis=0, keepdims=True)

    blk = pl.BlockSpec((t, HEAD_A), lambda j: (0, j))
    taps = pl.BlockSpec((CONV_A, HEAD_A), lambda j: (0, j))
    return pl.pallas_call(
        body, name=name, grid=(cols // HEAD_A,), in_specs=[blk, taps, blk], out_specs=[blk, taps],
        out_shape=[jax.ShapeDtypeStruct((t, cols), BF16), jax.ShapeDtypeStruct((CONV_A, cols), F32)],
        compiler_params=_params(("parallel",)))(h, w, dy)


def _softplus(x):
    return jnp.maximum(x, 0.0) + jnp.log1p(jnp.exp(-jnp.abs(x)))


def _gates_fwd(ba, arow, brow, name):
    t = ba.shape[0]

    def body(x_ref, a_ref, b_ref, o_ref):
        xv = x_ref[...]
        lane = lax.broadcasted_iota(jnp.int32, xv.shape, 1)
        beta = _sigmoid(xv)
        g = -jnp.exp(a_ref[...]) * _softplus(xv + b_ref[...])
        o_ref[...] = jnp.where(lane < N_HEADS_A, beta, jnp.where(lane < 2 * N_HEADS_A, g, 0.0))

    return pl.pallas_call(body, name=name, out_shape=jax.ShapeDtypeStruct((t, LANE), F32),
                          compiler_params=_params())(ba, arow, brow)


def _gates_bwd(ba, arow, brow, dgb, name):
    t = ba.shape[0]

    def body(x_ref, a_ref, b_ref, d_ref, dx_ref, da_ref, db_ref):
        xv = x_ref[...]
        dv = d_ref[0] + d_ref[1] + d_ref[2] + d_ref[3]
        lane = lax.broadcasted_iota(jnp.int32, xv.shape, 1)
        beta = _sigmoid(xv)
        ea = jnp.exp(a_ref[...])
        z = xv + b_ref[...]
        dgv = jnp.where((lane >= N_HEADS_A) & (lane < 2 * N_HEADS_A), dv, 0.0) * (-ea)
        dz = dgv * _sigmoid(z)
        dx = jnp.where(lane < N_HEADS_A, dv * beta * (1.0 - beta), dz)
        dx_ref[...] = dx.astype(BF16)
        db_ref[...] = jnp.sum(dz, axis=0, keepdims=True)
        da_ref[...] = jnp.sum(dgv * _softplus(z), axis=0, keepdims=True)

    return pl.pallas_call(
        body, name=name,
        out_shape=[jax.ShapeDtypeStruct((t, LANE), BF16), jax.ShapeDtypeStruct((1, LANE), F32),
                   jax.ShapeDtypeStruct((1, LANE), F32)],
        compiler_params=_params())(ba, arow, brow, dgb)


def _head_gates(gates, head):
    lane = lax.broadcasted_iota(jnp.int32, gates.shape, 1)
    beta = jnp.sum(jnp.where(lane == head, gates, 0.0), axis=1, keepdims=True)
    g = jnp.sum(jnp.where(lane == head + N_HEADS_A, gates, 0.0), axis=1, keepdims=True)
    return beta, g


_B_NN, _B_NT, _B_TN = ((2,), (1,)), ((2,), (2,)), ((1,), (1,))


def _bdot(a, b, dims=_B_NN, prec=None):
    if prec is None:
        a, b = a.astype(BF16), b.astype(BF16)
    return lax.dot_general(a, b, (dims, ((0,), (0,))), precision=prec, preferred_element_type=F32)


def _heads_of(ref):
    return jnp.stack([ref[:, h * HEAD_A:(h + 1) * HEAD_A] for h in range(N_HEADS_A)])


def _all_head_gates(gates):
    pairs = [_head_gates(gates, h) for h in range(N_HEADS_A)]
    return jnp.stack([b for b, _ in pairs]), jnp.stack([g for _, g in pairs])


def _gdr_terms(k, beta, g):
    h, c = k.shape[0], GDR_CHUNK
    row = lax.broadcasted_iota(jnp.int32, (c, c), 0)
    col = lax.broadcasted_iota(jnp.int32, (c, c), 1)
    causal, strict = row >= col, row > col
    lower = jnp.broadcast_to(causal.astype(F32), (h, c, c))
    gcum = _bdot(lower, jnp.broadcast_to(g, (h, c, c)), prec=HIGHEST)
    diff = gcum - jnp.swapaxes(gcum, 1, 2)
    decay = jnp.where(causal, jnp.exp(jnp.where(causal, diff, 0.0)), 0.0)
    kb = k * beta
    return row, col, causal, strict, gcum, decay, kb, _bdot(kb, k, _B_NT)


def _unit_lower_inverses(a):
    c = a.shape[1]
    eye = (lax.broadcasted_iota(jnp.int32, (c, c), 0) == lax.broadcasted_iota(jnp.int32, (c, c), 1)).astype(F32)
    p = -a
    inv = eye + p
    step = 1
    while 2 * step < c:
        p = _bdot(p, p, prec=HIGH)
        inv = inv + _bdot(inv, p, prec=HIGH)
        step *= 2
    return inv


def _gdr_fwd(qkv, gates, name, comm=None):
    t = qkv.shape[0]
    c, nh = GDR_CHUNK, N_HEADS_A
    n = t // c

    def body(q_ref, k_ref, v_ref, gb_ref, o_ref, tm_ref, s_ref, state):
        @pl.when(pl.program_id(0) == 0)
        def _():
            state[...] = jnp.zeros_like(state)

        qv, kv, vv = _heads_of(q_ref), _heads_of(k_ref), _heads_of(v_ref)
        beta, g = _all_head_gates(gb_ref[...])
        row, col, causal, strict, gcum, decay, kb, kk = _gdr_terms(kv, beta, g)
        tm = _unit_lower_inverses(jnp.where(strict, kk * decay, 0.0))
        e = jnp.exp(gcum)
        u = _bdot(tm, vv * beta, prec=HIGH)
        w = _bdot(tm, kb * e, prec=HIGH)
        p = jnp.where(causal, _bdot(qv, kv, _B_NT) * decay, 0.0)
        s = state[...]
        s_ref[:, 0] = s
        tm_ref[:, 0] = tm
        vn = u - _bdot(w, s)
        o = _bdot(qv * e, s) + _bdot(p, vn)
        for h in range(nh):
            o_ref[:, h * HEAD_A:(h + 1) * HEAD_A] = o[h]
        glast = gcum[:, c - 1:c, :]
        state[...] = s * jnp.exp(glast) + _bdot(kv * jnp.exp(glast - gcum), vn, _B_TN)

    part = lambda p: pl.BlockSpec((c, WIDTH_A), lambda i: (i, p))
    mat = pl.BlockSpec((nh, 1, c, c), lambda i: (0, i, 0, 0))
    return _call(
        body, name=name, grid=(n,), in_specs=[part(0), part(1), part(2), pl.BlockSpec((c, LANE), lambda i: (i, 0))],
        out_specs=[part(0), mat, mat],
        out_shape=[jax.ShapeDtypeStruct((t, WIDTH_A), F32), jax.ShapeDtypeStruct((nh, n, c, c), F32),
                   jax.ShapeDtypeStruct((nh, n, HEAD_A, HEAD_A), F32)],
        scratch_shapes=[pltpu.VMEM((nh, HEAD_A, HEAD_A), F32)], sem=("arbitrary",),
        args=(qkv, qkv, qkv, gates), comm=comm)


def _gdr_bwd(qkv, gates, tm_all, s_all, do, name, comm=None):
    t = qkv.shape[0]
    c, nh = GDR_CHUNK, N_HEADS_A
    n = t // c

    def body(q_ref, k_ref, v_ref, gb_ref, tm_ref, s_ref, do_ref, dqkv_ref, dgb_ref, dstate):
        @pl.when(pl.program_id(0) == 0)
        def _():
            dstate[...] = jnp.zeros_like(dstate)

        qv, kv, vv, dov = _heads_of(q_ref), _heads_of(k_ref), _heads_of(v_ref), _heads_of(do_ref)
        beta, g = _all_head_gates(gb_ref[...])
        tm, s, dsp = tm_ref[:, 0], s_ref[:, 0], dstate[...]
        row, col, causal, strict, gcum, decay, kb, kk = _gdr_terms(kv, beta, g)
        rowsum = lambda x: jnp.sum(x, axis=2, keepdims=True)
        e = jnp.exp(gcum)
        vb, kbe = vv * beta, kb * e
        u = _bdot(tm, vb, prec=HIGH)
        w = _bdot(tm, kbe, prec=HIGH)
        qk = _bdot(qv, kv, _B_NT)
        p = jnp.where(causal, qk * decay, 0.0)
        vn = u - _bdot(w, s)
        glast = gcum[:, c - 1:c, :]
        el = jnp.exp(glast)
        f = jnp.exp(glast - gcum)
        kd = kv * f
        qe = qv * e

        dvn = _bdot(p, dov, _B_TN) + _bdot(kd, dsp)
        dglast = el[:, :, 0:1] * jnp.sum(s * dsp, axis=(1, 2), keepdims=True)
        dkd = _bdot(vn, dsp, _B_NT)
        dk = dkd * f
        df = rowsum(dkd * kv) * f[:, :, 0:1]
        dglast = dglast + jnp.sum(df, axis=1, keepdims=True)
        dgc = -df
        dp = jnp.where(causal, _bdot(dov, vn, _B_NT), 0.0)
        dqe = _bdot(dov, s, _B_NT)
        dq = dqe * e
        de = rowsum(dqe * qv)
        dstate[...] = dsp * el + _bdot(qe, dov, _B_TN) - _bdot(w, dvn, _B_TN)
        dw = -_bdot(dvn, s, _B_NT)
        dvb = _bdot(tm, dvn, _B_TN, prec=HIGH)
        dkbe = _bdot(tm, dw, _B_TN, prec=HIGH)
        da = -jnp.where(strict, _bdot(dvb, u, _B_NT) + _bdot(dkbe, w, _B_NT), 0.0)
        dkk = da * decay
        dqk = dp * decay
        dd = da * kk + dp * qk
        dq = dq + _bdot(dqk, kv)
        dk = dk + _bdot(dqk, qv, _B_TN)
        dkb = _bdot(dkk, kv) + dkbe * e
        dk = dk + _bdot(dkk, kb, _B_TN)
        de = de + rowsum(dkbe * kb)
        dk = dk + dkb * beta
        dbeta = rowsum(dkb * kv) + rowsum(dvb * vv)
        m = dd * decay
        dgc = dgc + rowsum(m) - rowsum(jnp.swapaxes(m, 1, 2))
        dgc = dgc + de * e[:, :, 0:1]
        dgc = dgc + jnp.where(row[:, 0:1] == c - 1, dglast, 0.0)
        upper = jnp.broadcast_to((row <= col).astype(F32), (nh, c, c))
        dg = _bdot(upper, jnp.broadcast_to(dgc, (nh, c, c)), prec=HIGHEST)
        dv = dvb * beta
        for p, grad in enumerate((dq, dk, dv)):
            for h in range(nh):
                dqkv_ref[:, p * WIDTH_A + h * HEAD_A:p * WIDTH_A + (h + 1) * HEAD_A] = grad[h]
        head = lax.broadcasted_iota(jnp.int32, (nh, c, LANE), 0)
        lane = lax.broadcasted_iota(jnp.int32, (nh, c, LANE), 2)
        dgb_ref[...] = jnp.where(lane == head, dbeta, jnp.where(lane == head + nh, dg, 0.0))

    part = lambda p: pl.BlockSpec((c, WIDTH_A), lambda i: (n - 1 - i, p))
    mat = pl.BlockSpec((nh, 1, c, c), lambda i: (0, n - 1 - i, 0, 0))
    return _call(
        body, name=name, grid=(n,),
        in_specs=[part(0), part(1), part(2), pl.BlockSpec((c, LANE), lambda i: (n - 1 - i, 0)), mat, mat, part(0)],
        out_specs=[pl.BlockSpec((c, 3 * WIDTH_A), lambda i: (n - 1 - i, 0)),
                   pl.BlockSpec((nh, c, LANE), lambda i: (0, n - 1 - i, 0))],
        out_shape=[jax.ShapeDtypeStruct((t, 3 * WIDTH_A), F32), jax.ShapeDtypeStruct((nh, t, LANE), F32)],
        scratch_shapes=[pltpu.VMEM((nh, HEAD_A, HEAD_A), F32)], sem=("arbitrary",),
        args=(qkv, qkv, qkv, gates, tm_all, s_all, do), comm=comm)


def _onorm_fwd(o, gate, g, name):
    t = o.shape[0]

    def body(o_ref, gate_ref, g_ref, y_ref):
        ov, gv = o_ref[...], gate_ref[...]
        r = lax.rsqrt(jnp.mean(ov * ov, axis=-1, keepdims=True) + RMS_EPS)
        y_ref[...] = (ov * r * g_ref[...] * gv * _sigmoid(gv)).astype(BF16)

    blk = pl.BlockSpec((t, HEAD_A), lambda j: (0, j))
    return pl.pallas_call(
        body, name=name, grid=(N_HEADS_A,), in_specs=[blk, blk, pl.BlockSpec((1, HEAD_A), lambda j: (0, 0))],
        out_specs=blk, out_shape=jax.ShapeDtypeStruct((t, WIDTH_A), BF16),
        compiler_params=_params(("parallel",)))(o, gate, g)


def _onorm_bwd(o, gate, g, dy, name):
    t = o.shape[0]

    def body(o_ref, gate_ref, g_ref, dy_ref, do_ref, dgate_ref, dg_ref):
        @pl.when(pl.program_id(0) == 0)
        def _():
            dg_ref[...] = jnp.zeros_like(dg_ref)

        ov, gv, dyv = o_ref[...], gate_ref[...], dy_ref[...].astype(F32)
        r = lax.rsqrt(jnp.mean(ov * ov, axis=-1, keepdims=True) + RMS_EPS)
        oh = ov * r
        sg, dsg = _silu_and_grad(gv)
        dgate_ref[...] = (dyv * oh * g_ref[...] * dsg).astype(BF16)
        dn = dyv * sg
        dg_ref[...] += jnp.sum(dn * oh, axis=0, keepdims=True)
        dng = dn * g_ref[...]
        do_ref[...] = r * (dng - oh * jnp.mean(dng * oh, axis=-1, keepdims=True))

    blk = pl.BlockSpec((t, HEAD_A), lambda j: (0, j))
    vec = pl.BlockSpec((1, HEAD_A), lambda j: (0, 0))
    return pl.pallas_call(
        body, name=name, grid=(N_HEADS_A,), in_specs=[blk, blk, vec, blk], out_specs=[blk, blk, vec],
        out_shape=[jax.ShapeDtypeStruct((t, WIDTH_A), F32), jax.ShapeDtypeStruct((t, WIDTH_A), BF16),
                   jax.ShapeDtypeStruct((1, HEAD_A), F32)],
        compiler_params=_params(("arbitrary",)))(o, gate, g, dy)


def _cmul(ar, ai, br, bi):
    return ar * br - ai * bi, ar * bi + ai * br


def _scan_tables(ar, ai, reverse):
    p1 = (ar, ai)
    p2 = _cmul(*p1, *p1)
    p4 = _cmul(*p2, *p2)
    p8 = _cmul(*p4, *p4)
    p3 = _cmul(*p2, *p1)
    p5 = _cmul(*p4, *p1)
    p6 = _cmul(*p4, *p2)
    p7 = _cmul(*p4, *p3)
    pows = [p1, p2, p3, p4, p5, p6, p7, p8]
    rows = lax.broadcasted_iota(jnp.int32, (8, ar.shape[1]), 0)
    tr = jnp.zeros((8, ar.shape[1]), F32)
    ti = jnp.zeros((8, ar.shape[1]), F32)
    for r in range(8):
        pw = pows[7 - r] if reverse else pows[r]
        tr = jnp.where(rows == r, pw[0], tr)
        ti = jnp.where(rows == r, pw[1], ti)
    return p1, p2, p4, p8, tr, ti


def _tile_scan(xr, xi, p1, p2, p4, reverse):
    rows = lax.broadcasted_iota(jnp.int32, xr.shape, 0)
    for s, (pr, pi) in ((1, p1), (2, p2), (4, p4)):
        if reverse:
            keep = rows < 8 - s
            sr, si = pltpu.roll(xr, 8 - s, 0), pltpu.roll(xi, 8 - s, 0)
        else:
            keep = rows >= s
            sr, si = pltpu.roll(xr, s, 0), pltpu.roll(xi, s, 0)
        sr, si = jnp.where(keep, sr, 0.0), jnp.where(keep, si, 0.0)
        mr, mi = _cmul(pr, pi, sr, si)
        xr, xi = xr + mr, xi + mi
    return xr, xi


def _s5_scan_fwd(bu, a, name, tb=512, comm=None):
    t = bu.shape[0]
    cb = SCAN_CB
    nt = t // tb

    def body(b_ref, a_ref, x_ref, carry):
        @pl.when(pl.program_id(1) == 0)
        def _():
            carry[...] = jnp.zeros_like(carry)

        ar, ai = a_ref[:, 0:cb], a_ref[:, cb:2 * cb]
        p1, p2, p4, p8, tr, ti = _scan_tables(ar, ai, False)

        def step(j, c):
            cr, ci = c
            i = pl.multiple_of(j * 8, 8)
            xr, xi = _tile_scan(b_ref[pl.ds(i, 8), 0:cb], b_ref[pl.ds(i, 8), cb:2 * cb], p1, p2, p4, False)
            mr, mi = _cmul(tr, ti, cr, ci)
            xr, xi = xr + mr, xi + mi
            x_ref[pl.ds(i, 8), 0:cb] = xr
            x_ref[pl.ds(i, 8), cb:2 * cb] = xi
            return xr[7:8, :], xi[7:8, :]

        cr, ci = lax.fori_loop(0, tb // 8, step, (carry[0:1, :], carry[1:2, :]), unroll=2)
        carry[0:1, :] = cr
        carry[1:2, :] = ci

    blk = pl.BlockSpec((tb, 2 * cb), lambda j, i: (i, j))
    return _call(
        body, name=name, grid=(SSM_CH // cb, nt),
        in_specs=[blk, pl.BlockSpec((1, 2 * cb), lambda j, i: (0, j))], out_specs=blk,
        out_shape=jax.ShapeDtypeStruct((t, 2 * SSM_CH), F32), scratch_shapes=[pltpu.VMEM((8, cb), F32)],
        sem=("parallel", "arbitrary"), args=(bu, a), comm=comm)


def _s5_scan_bwd(dx, x, a, name, tb=512, comm=None):
    t = dx.shape[0]
    cb = SCAN_CB
    nt = t // tb
    nj = tb // 8

    def body(d_ref, x_ref, xp_ref, a_ref, l_ref, da_ref, carry, acc):
        tblk = pl.program_id(1)

        @pl.when(tblk == 0)
        def _():
            carry[...] = jnp.zeros_like(carry)
            acc[...] = jnp.zeros_like(acc)

        ar, ai = a_ref[:, 0:cb], a_ref[:, cb:2 * cb]
        p1, p2, p4, p8, tr, ti = _scan_tables(ar, -ai, True)
        rows = lax.broadcasted_iota(jnp.int32, (8, cb), 0)

        def step(jj, c):
            cr, ci, sr_acc, si_acc = c
            j = nj - 1 - jj
            i = pl.multiple_of(j * 8, 8)
            lr, li = _tile_scan(d_ref[pl.ds(i, 8), 0:cb], d_ref[pl.ds(i, 8), cb:2 * cb], p1, p2, p4, True)
            mr, mi = _cmul(tr, ti, cr, ci)
            lr, li = lr + mr, li + mi
            l_ref[pl.ds(i, 8), 0:cb] = lr
            l_ref[pl.ds(i, 8), cb:2 * cb] = li
            ip = pl.multiple_of(jnp.maximum(j - 1, 0) * 8, 8)
            prev_r = jnp.where(j > 0, x_ref[pl.ds(ip, 8), 0:cb], xp_ref[:, 0:cb])
            prev_i = jnp.where(j > 0, x_ref[pl.ds(ip, 8), cb:2 * cb], xp_ref[:, cb:2 * cb])
            edge = jnp.where(jnp.logical_and(j == 0, tblk == nt - 1), 0.0, 1.0)
            xs_r = jnp.where(rows == 0, pltpu.roll(prev_r, 1, 0) * edge, pltpu.roll(x_ref[pl.ds(i, 8), 0:cb], 1, 0))
            xs_i = jnp.where(rows == 0, pltpu.roll(prev_i, 1, 0) * edge, pltpu.roll(x_ref[pl.ds(i, 8), cb:2 * cb], 1, 0))
            sr_acc = sr_acc + lr * xs_r + li * xs_i
            si_acc = si_acc + li * xs_r - lr * xs_i
            return lr[0:1, :], li[0:1, :], sr_acc, si_acc

        cr, ci, sr_acc, si_acc = lax.fori_loop(
            0, nj, step, (carry[0:1, :], carry[1:2, :], acc[:, 0:cb], acc[:, cb:2 * cb]))
        carry[0:1, :] = cr
        carry[1:2, :] = ci
        acc[:, 0:cb] = sr_acc
        acc[:, cb:2 * cb] = si_acc

        @pl.when(tblk == nt - 1)
        def _():
            da_ref[...] = jnp.sum(acc[...], axis=0, keepdims=True)

    blk = pl.BlockSpec((tb, 2 * cb), lambda j, i: (nt - 1 - i, j))
    prev = pl.BlockSpec((8, 2 * cb), lambda j, i: (jnp.maximum((nt - 1 - i) * (tb // 8) - 1, 0), j))
    vec = pl.BlockSpec((1, 2 * cb), lambda j, i: (0, j))
    return _call(
        body, name=name, grid=(SSM_CH // cb, nt), in_specs=[blk, blk, prev, vec], out_specs=[blk, vec],
        out_shape=[jax.ShapeDtypeStruct((t, 2 * SSM_CH), F32), jax.ShapeDtypeStruct((1, 2 * SSM_CH), F32)],
        scratch_shapes=[pltpu.VMEM((8, cb), F32), pltpu.VMEM((8, 2 * cb), F32)],
        sem=("parallel", "arbitrary"), args=(dx, x, x, a), comm=comm)


def _glu_fwd(yc, u, dvec, wg, bg, name, tr=256):
    t = yc.shape[0]

    def body(yc_ref, u_ref, d_ref, w_ref, b_ref, yl_ref, yb_ref):
        yl = yc_ref[...] + d_ref[...] * u_ref[...]
        yl_ref[...] = yl
        yg, _ = _gelu_and_grad(yl)
        z = jnp.dot(yg.astype(BF16), w_ref[...], preferred_element_type=F32) + b_ref[...]
        yb_ref[...] = (yg * _sigmoid(z)).astype(BF16)

    blk = pl.BlockSpec((tr, SSM_WIDTH), lambda i: (i, 0))
    vec = pl.BlockSpec((1, SSM_WIDTH), lambda i: (0, 0))
    return pl.pallas_call(
        body, name=name, grid=(t // tr,),
        in_specs=[blk, blk, vec, pl.BlockSpec((SSM_WIDTH, SSM_WIDTH), lambda i: (0, 0)), vec],
        out_specs=[blk, blk],
        out_shape=[jax.ShapeDtypeStruct((t, SSM_WIDTH), F32), jax.ShapeDtypeStruct((t, SSM_WIDTH), BF16)],
        compiler_params=_params(("parallel",)))(yc, u, dvec, wg, bg)


def _glu_bwd(yl, u, dvec, wg, bg, dyb, name, tr=256):
    t = yl.shape[0]

    def body(yl_ref, u_ref, d_ref, w_ref, b_ref, dy_ref, dyl_ref, du_ref, dw_ref, db_ref, dd_ref):
        @pl.when(pl.program_id(0) == 0)
        def _():
            dw_ref[...] = jnp.zeros_like(dw_ref)
            db_ref[...] = jnp.zeros_like(db_ref)
            dd_ref[...] = jnp.zeros_like(dd_ref)

        ylv, dyv, wv = yl_ref[...], dy_ref[...].astype(F32), w_ref[...]
        yg, dgelu = _gelu_and_grad(ylv)
        ygb = yg.astype(BF16)
        z = jnp.dot(ygb, wv, preferred_element_type=F32) + b_ref[...]
        sg = _sigmoid(z)
        dz = dyv * yg * sg * (1.0 - sg)
        dzb = dz.astype(BF16)
        dyg = dyv * sg + lax.dot_general(dzb, wv, (((1,), (1,)), ((), ())), preferred_element_type=F32)
        dyl = dyg * dgelu
        dyl_ref[...] = dyl.astype(BF16)
        du_ref[...] = dyl * d_ref[...]
        dw_ref[...] += lax.dot_general(ygb, dzb, (((0,), (0,)), ((), ())), preferred_element_type=F32)
        db_ref[...] += jnp.sum(dz, axis=0, keepdims=True)
        dd_ref[...] += jnp.sum(dyl * u_ref[...], axis=0, keepdims=True)

    blk = pl.BlockSpec((tr, SSM_WIDTH), lambda i: (i, 0))
    vec = pl.BlockSpec((1, SSM_WIDTH), lambda i: (0, 0))
    wsp = pl.BlockSpec((SSM_WIDTH, SSM_WIDTH), lambda i: (0, 0))
    return pl.pallas_call(
        body, name=name, grid=(t // tr,), in_specs=[blk, blk, vec, wsp, vec, blk],
        out_specs=[blk, blk, wsp, vec, vec],
        out_shape=[jax.ShapeDtypeStruct((t, SSM_WIDTH), BF16), jax.ShapeDtypeStruct((t, SSM_WIDTH), F32),
                   jax.ShapeDtypeStruct((SSM_WIDTH, SSM_WIDTH), F32), jax.ShapeDtypeStruct((1, SSM_WIDTH), F32),
                   jax.ShapeDtypeStruct((1, SSM_WIDTH), F32)],
        compiler_params=_params(("arbitrary",)))(yl, u, dvec, wg, bg, dyb)


def _mesh_pos():
    return lax.axis_index("x"), lax.axis_index("y"), lax.axis_index("c")


def _device_index():
    x, y, c = _mesh_pos()
    return 4 * x + 2 * y + c


def _gather_comm(arrays):
    na = len(arrays)

    def own_copy(ins, outs, sems, ai):
        return pltpu.make_async_copy(ins[ai], outs[ai].at[_device_index()], sems[2].at[ai])

    def ctx(ins, outs, sems):
        send_sems, recv_sems = sems[:2]
        x, y, c = _mesh_pos()
        chips = [(1 - x, y), (x, 1 - y), (1 - x, 1 - y)]

        def copy(ai, kk, block, to, own=False):
            slot = outs[ai].at[4 * block[0] + 2 * block[1] + block[2]]
            return pltpu.make_async_remote_copy(
                src_ref=ins[ai] if own else slot, dst_ref=slot, send_sem=send_sems.at[ai, kk],
                recv_sem=recv_sems.at[ai, kk], device_id=to, device_id_type=MESH)

        return (x, y, c), (x, y, 1 - c), chips, c, copy

    def start(ins, outs, sems):
        me, sibling, chips, c, copy = ctx(ins, outs, sems)
        for ai in range(na):
            copy(ai, 0, me, sibling, own=True).start()
            for j, chip in enumerate(chips):
                copy(ai, 1 + j, me, (*chip, c), own=True).start()
        for ai in range(na):
            own_copy(ins, outs, sems, ai).start()

    def mid(ins, outs, sems):
        me, sibling, chips, c, copy = ctx(ins, outs, sems)
        for ai in range(na):
            for j, chip in enumerate(chips):
                copy(ai, 1 + j, (*chip, c), me).wait_recv()
                copy(ai, 4 + j, (*chip, c), sibling).start()

    def end(ins, outs, sems):
        me, sibling, chips, c, copy = ctx(ins, outs, sems)
        for ai in range(na):
            copy(ai, 0, sibling, me).wait_recv()
            copy(ai, 0, me, sibling, own=True).wait_send()
            for j, chip in enumerate(chips):
                copy(ai, 4 + j, (*chip, 1 - c), me).wait_recv()
                copy(ai, 1 + j, me, (*chip, c), own=True).wait_send()
                copy(ai, 4 + j, (*chip, c), sibling).wait_send()
            own_copy(ins, outs, sems, ai).wait()

    return Comm(arrays, [jax.ShapeDtypeStruct((N_DEV,) + a.shape, a.dtype) for a in arrays],
                [pltpu.SemaphoreType.DMA((na, 7)), pltpu.SemaphoreType.DMA((na, 7)), pltpu.SemaphoreType.DMA((na,))],
                start, end, mid)


def _sequencer_gather(arrays, name, collective_id):
    comm = _gather_comm(arrays)
    na = len(arrays)

    def body(*refs):
        ins, outs, sems = refs[:na], refs[na:2 * na], refs[2 * na:]
        x, y, c = _mesh_pos()
        peers = [(x, y, 1 - c), (1 - x, y, c), (x, 1 - y, c), (1 - x, 1 - y, c)]
        barrier = pltpu.get_barrier_semaphore()
        for peer in peers:
            pl.semaphore_signal(barrier, inc=1, device_id=peer, device_id_type=MESH)
        pl.semaphore_wait(barrier, len(peers))
        comm.start(ins, outs, sems)
        comm.mid(ins, outs, sems)
        comm.end(ins, outs, sems)

    return list(pl.kernel(
        body, out_type=tuple(comm.out_shapes), mesh=plsc.ScalarSubcoreMesh(axis_name="sequencer", num_cores=1),
        name=name, scratch_types=tuple(comm.sems),
        compiler_params=pltpu.CompilerParams(collective_id=collective_id))(*arrays))


def _sequencer_exchange(comm, peers_of, name, collective_id):
    na = len(comm.inputs)

    def body(*refs):
        ins, outs, sems = refs[:na], refs[na:na + len(comm.out_shapes)], refs[na + len(comm.out_shapes):]
        peers = peers_of(*_mesh_pos())
        barrier = pltpu.get_barrier_semaphore()
        for peer in peers:
            pl.semaphore_signal(barrier, inc=1, device_id=peer, device_id_type=MESH)
        pl.semaphore_wait(barrier, len(peers))
        comm.start(ins, outs, sems)
        comm.end(ins, outs, sems)

    return list(pl.kernel(
        body, out_type=tuple(comm.out_shapes), mesh=plsc.ScalarSubcoreMesh(axis_name="sequencer", num_cores=1),
        name=name, scratch_types=tuple(comm.sems),
        compiler_params=pltpu.CompilerParams(collective_id=collective_id))(*comm.inputs))


SIBLING_SWAP_ID, CHIP_EXCHANGE_ID = 9, 10


def _sequencer_swap(arrays, name):
    return _sequencer_exchange(_swap_comm(arrays), lambda x, y, c: [(x, y, 1 - c)], name, SIBLING_SWAP_ID)[0]


def _sequencer_chips(send, name):
    return _sequencer_exchange(_chips_comm(send), lambda x, y, c: [(1 - x, y, c), (x, 1 - y, c), (1 - x, 1 - y, c)],
                               name, CHIP_EXCHANGE_ID)[0]


def _swap_comm(arrays):
    na = len(arrays)
    offs = np.concatenate([[0], np.cumsum([a.shape[1] for a in arrays])]).astype(int)

    def copies(ins, outs, sems):
        x, y, c = _mesh_pos()
        return [pltpu.make_async_remote_copy(
            src_ref=ins[ai].at[2 * k + 1 - c], dst_ref=outs[0].at[k, pl.ds(int(offs[ai]), arrays[ai].shape[1])],
            send_sem=sems[0].at[ai, k], recv_sem=sems[1].at[ai, k], device_id=(x, y, 1 - c), device_id_type=MESH)
            for ai in range(na) for k in range(4)]

    def start(ins, outs, sems):
        for cp in copies(ins, outs, sems):
            cp.start()

    def end(ins, outs, sems):
        for cp in copies(ins, outs, sems):
            cp.wait()

    return Comm(arrays, [jax.ShapeDtypeStruct((4, int(offs[-1]), PACK_COLS), arrays[0].dtype)],
                [pltpu.SemaphoreType.DMA((na, 4)), pltpu.SemaphoreType.DMA((na, 4))], start, end)


def _chips_comm(send):
    def copies(ins, outs, sems):
        x, y, c = _mesh_pos()
        chips = [(1 - x, y), (x, 1 - y), (1 - x, 1 - y)]
        return [pltpu.make_async_remote_copy(
            src_ref=ins[0].at[2 * cx + cy], dst_ref=outs[0].at[j], send_sem=sems[0].at[j], recv_sem=sems[1].at[j],
            device_id=(cx, cy, c), device_id_type=MESH) for j, (cx, cy) in enumerate(chips)]

    def start(ins, outs, sems):
        for cp in copies(ins, outs, sems):
            cp.start()

    def end(ins, outs, sems):
        for cp in copies(ins, outs, sems):
            cp.wait()

    return Comm([send], [jax.ShapeDtypeStruct((3,) + send.shape[1:], send.dtype)],
                [pltpu.SemaphoreType.DMA((3,)), pltpu.SemaphoreType.DMA((3,))], start, end)


def _pair_sum(keep, recv, name, tr=464):
    nchip, rows, cols = keep.shape

    def body(g_ref, r_ref, o_ref):
        o_ref[...] = (g_ref[...].astype(F32) + r_ref[...].astype(F32)).astype(BF16)

    blk = pl.BlockSpec((1, tr, cols), lambda k, i: (k, i, 0))
    return pl.pallas_call(
        body, name=name, grid=(nchip, rows // tr), in_specs=[blk, blk], out_specs=blk,
        out_shape=jax.ShapeDtypeStruct((nchip, rows, cols), BF16),
        compiler_params=_params(("parallel", "parallel")))(keep, recv)


def _pair_sum_pieces(pieces, recv, name, tr):
    _, rows, cols = pieces.shape
    core = lax.axis_index("c").astype(jnp.int32).reshape(1)

    def body(c_ref, g_ref, r_ref, o_ref):
        del c_ref
        o_ref[...] = (g_ref[...].astype(F32) + r_ref[...].astype(F32)).astype(BF16)

    grid_spec = pltpu.PrefetchScalarGridSpec(
        num_scalar_prefetch=1, grid=(4, rows // tr),
        in_specs=[pl.BlockSpec((1, tr, cols), lambda k, i, c_ref: (2 * k + c_ref[0], i, 0)),
                  pl.BlockSpec((1, tr, cols), lambda k, i, c_ref: (k, i, 0))],
        out_specs=pl.BlockSpec((1, tr, cols), lambda k, i, c_ref: (k, i, 0)))
    return pl.pallas_call(
        body, name=name, grid_spec=grid_spec, out_shape=jax.ShapeDtypeStruct((4, rows, cols), BF16),
        compiler_params=_params(("parallel", "parallel")))(core, pieces, recv)


def _chip_sum(own, others, name, tr=464):
    _, rows, cols = own.shape
    chip = (2 * lax.axis_index("x") + lax.axis_index("y")).astype(jnp.int32).reshape(1)

    def body(chip_ref, own_ref, oth_ref, o_ref):
        del chip_ref
        acc = own_ref[0].astype(F32)
        for j in range(3):
            acc = acc + oth_ref[j].astype(F32)
        o_ref[...] = acc

    grid_spec = pltpu.PrefetchScalarGridSpec(
        num_scalar_prefetch=1, grid=(rows // tr,),
        in_specs=[pl.BlockSpec((1, tr, cols), lambda i, chip_ref: (chip_ref[0], i, 0)),
                  pl.BlockSpec((3, tr, cols), lambda i, chip_ref: (0, i, 0))],
        out_specs=pl.BlockSpec((tr, cols), lambda i, chip_ref: (i, 0)))
    return pl.pallas_call(
        body, name=name, grid_spec=grid_spec, out_shape=jax.ShapeDtypeStruct((rows, cols), F32),
        compiler_params=_params(("parallel",)))(chip, own, others)


def _sum_leading(parts, name, tr=464):
    nparts, rows, cols = parts.shape
    tr = tr if rows % tr == 0 else rows

    def body(p_ref, o_ref):
        acc = p_ref[0].astype(F32)
        for i in range(1, nparts):
            acc = acc + p_ref[i].astype(F32)
        o_ref[...] = acc

    return pl.pallas_call(
        body, name=name, grid=(rows // tr,),
        in_specs=[pl.BlockSpec((nparts, tr, cols), lambda i: (0, i, 0))],
        out_specs=pl.BlockSpec((tr, cols), lambda i: (i, 0)), out_shape=jax.ShapeDtypeStruct((rows, cols), F32),
        compiler_params=_params(("parallel",)))(parts)


def _adamw(w, g, m, v, name, comm=None):
    shape = w.shape
    cols = shape[-1]
    lead = shape[0] if len(shape) >= 3 else 1
    rows = int(np.prod(shape[:-1])) // lead if len(shape) > 1 else 1
    w2, g2, m2, v2 = (a.reshape(lead, rows, cols) for a in (w, g, m, v))
    tr = rows
    for cand in (512, 256, 128, 64, 32, 16, 8):
        if rows % cand == 0 and rows > cand:
            tr = cand
            break
    bc1, bc2 = 1.0 - ADAM_B1 ** ADAM_STEP, 1.0 - ADAM_B2 ** ADAM_STEP

    def body(w_ref, g_ref, m_ref, v_ref, d_ref, nm_ref, nv_ref):
        gv = g_ref[...]
        nm = ADAM_B1 * m_ref[...] + (1.0 - ADAM_B1) * gv
        nv = ADAM_B2 * v_ref[...] + (1.0 - ADAM_B2) * (gv * gv)
        nm_ref[...] = nm
        nv_ref[...] = nv
        d_ref[...] = -ADAM_LR * ((nm / bc1) / (jnp.sqrt(nv / bc2) + ADAM_EPS) + ADAM_WD * w_ref[...])

    blk = pl.BlockSpec((1, tr, cols), lambda l, i: (l, i, 0))
    res = _call(body, name=name, grid=(lead, rows // tr), in_specs=[blk] * 4, out_specs=[blk] * 3,
                out_shape=[jax.ShapeDtypeStruct((lead, rows, cols), F32)] * 3, sem=("parallel", "parallel"),
                args=(w2, g2, m2, v2), comm=comm)
    outs, couts = res if comm is not None else (res, None)
    outs = tuple(o.reshape(shape) for o in outs)
    return outs if comm is None else (outs, couts)


WEIGHT_NAMES = ['norm_mix_g', 'norm_xa_g', 'norm_ffn_g', 'norm_mem_g', 'norm_final_g', 'w_in_ab', 'conv_qkv_a',
                'a_log_a', 'dt_bias_a', 'onorm_g_a', 'ssm_lambda_re', 'ssm_lambda_im', 'ssm_b_re', 'ssm_b_im',
                'ssm_c_re', 'ssm_c_im', 'ssm_d', 'ssm_log_dt', 'w_glu_b', 'b_glu_b', 'w_out_ab', 'pool_w',
                'pool_scale', 'xa_wq', 'xa_wkv', 'xa_wo', 'ffn_w_up', 'ffn_conv', 'ffn_w_down']
BIG_SHARDED = {'w_in_ab': ((1, 1024, 2568), 2), 'w_glu_b': ((1, 512, 512), 1), 'w_out_ab': ((1, 1024, 1024), 1),
               'pool_w': ((1, 4, 256, 256), 2), 'xa_wq': ((2, 1024, 1024), 1), 'xa_wkv': ((2, 1024, 2048), 2),
               'xa_wo': ((2, 1024, 1024), 1), 'ffn_w_up': ((2, 1024, 5632), 2), 'ffn_w_down': ((2, 2816, 1024), 1)}
SMALL_SHARDED = {'conv_qkv_a': ((1, 4, 1536), 2), 'pool_scale': ((1, 1024), 1), 'ffn_conv': ((2, 3, 5632), 2)}
REPLICATED = {'norm_mix_g': (2, 1024), 'norm_xa_g': (2, 1024), 'norm_ffn_g': (2, 1024), 'norm_mem_g': (1024,),
              'norm_final_g': (1024,), 'a_log_a': (1, 4), 'dt_bias_a': (1, 4), 'onorm_g_a': (1, 128),
              'ssm_lambda_re': (1, 32, 64), 'ssm_lambda_im': (1, 32, 64), 'ssm_b_re': (1, 32, 64, 16),
              'ssm_b_im': (1, 32, 64, 16), 'ssm_c_re': (1, 32, 16, 64), 'ssm_c_im': (1, 32, 16, 64),
              'ssm_d': (1, 32, 16), 'ssm_log_dt': (1, 32), 'b_glu_b': (1, 512)}
PACK_ROW_ALIGN = 8


def _shard_shape(shape, axis):
    return tuple(s // N_DEV if i == axis else s for i, s in enumerate(shape))


def _round_up(n, m):
    return (n + m - 1) // m * m


def _pack(arrays):
    total = sum(int(np.prod(a.shape)) for a in arrays)
    padded = _round_up(total, PACK_COLS * PACK_ROW_ALIGN)
    parts = [a.astype(F32).reshape(-1) for a in arrays]
    if padded != total:
        parts.append(jnp.zeros((padded - total,), F32))
    return jnp.concatenate(parts).reshape(padded // PACK_COLS, PACK_COLS)


def _unpack(packed, shapes):
    flat, out, off = packed.reshape(-1), [], 0
    for shape in shapes:
        size = int(np.prod(shape))
        out.append(flat[off:off + size].reshape(shape))
        off += size
    return out


def _split_shards(full, axis):
    shape = full.shape
    s = shape[axis] // N_DEV
    a = full.reshape(shape[:axis] + (N_DEV, s) + shape[axis + 1:])
    return jnp.moveaxis(a, axis, 0).reshape(N_DEV, -1)


def _merge_shards(pieces, shape, axis):
    sh = _shard_shape(shape, axis)
    a = pieces.reshape((N_DEV,) + sh)
    a = jnp.moveaxis(a, 0, axis)
    return a.reshape(shape)


_SCAN_NB = SSM_CH // SCAN_CB


def _to_scan_layout(m, axis):
    shape = m.shape
    m = m.reshape(shape[:axis] + (2, _SCAN_NB, SCAN_CB) + shape[axis + 1:])
    return jnp.swapaxes(m, axis, axis + 1).reshape(shape)


def _from_scan_layout(m, axis):
    shape = m.shape
    m = m.reshape(shape[:axis] + (_SCAN_NB, 2, SCAN_CB) + shape[axis + 1:])
    return jnp.swapaxes(m, axis, axis + 1).reshape(shape)


def _s5_discretise(lam_re, lam_im, b_re, b_im, log_dt):
    dt = jnp.exp(log_dt)[:, None]
    mag = jnp.exp(lam_re * dt)
    ang = lam_im * dt
    lb_re, lb_im = mag * jnp.cos(ang), mag * jnp.sin(ang)
    den = lam_re * lam_re + lam_im * lam_im
    nr, ni = lb_re - 1.0, lb_im
    coef_re = (nr * lam_re + ni * lam_im) / den
    coef_im = (ni * lam_re - nr * lam_im) / den
    bb_re = coef_re[..., None] * b_re - coef_im[..., None] * b_im
    bb_im = coef_re[..., None] * b_im + coef_im[..., None] * b_re
    return lb_re, lb_im, bb_re, bb_im


_GROUPS_PER_BLOCK = N_GROUPS // _SCAN_NB
_U_BLOCK = _GROUPS_PER_BLOCK * SSM_GROUP


def _s5_matrices(lb_re, lb_im, bb_re, bb_im, c_re, c_im):
    eye = jnp.eye(_GROUPS_PER_BLOCK, dtype=F32)
    blocked = lambda m: m.reshape((_SCAN_NB, _GROUPS_PER_BLOCK) + m.shape[1:])
    bmat = lambda bb: jnp.einsum('jgph,gk->jghkp', blocked(bb), eye).reshape(_SCAN_NB, _U_BLOCK, SCAN_CB)
    cmat = lambda cc: jnp.einsum('jghp,gk->jkpgh', blocked(cc), eye).reshape(_SCAN_NB, SCAN_CB, _U_BLOCK)
    b_in = jnp.concatenate([bmat(bb_re), bmat(bb_im)], axis=2)
    c_out = jnp.concatenate([cmat(c_re), -cmat(c_im)], axis=1)
    a_row = _to_scan_layout(jnp.concatenate([lb_re.reshape(1, SSM_CH), lb_im.reshape(1, SSM_CH)], axis=1), 1)
    return b_in, c_out, a_row


def _s5_matrix_grads(db_in, dc_out, da_row):
    da_nat = _from_scan_layout(da_row, 1)
    eye = jnp.eye(_GROUPS_PER_BLOCK, dtype=F32)
    nb, gb = _SCAN_NB, _GROUPS_PER_BLOCK
    bgrad = lambda m: jnp.einsum('jghkp,gk->jgph', m.reshape(nb, gb, SSM_GROUP, gb, SSM_STATE), eye
                                 ).reshape(N_GROUPS, SSM_STATE, SSM_GROUP)
    cgrad = lambda m: jnp.einsum('jkpgh,gk->jghp', m.reshape(nb, gb, SSM_STATE, gb, SSM_GROUP), eye
                                 ).reshape(N_GROUPS, SSM_GROUP, SSM_STATE)
    dbb_re, dbb_im = bgrad(db_in[:, :, :SCAN_CB]), bgrad(db_in[:, :, SCAN_CB:])
    dc_re, dc_im = cgrad(dc_out[:, :SCAN_CB]), -cgrad(dc_out[:, SCAN_CB:])
    dlb_re = da_nat[0, :SSM_CH].reshape(N_GROUPS, SSM_STATE)
    dlb_im = da_nat[0, SSM_CH:].reshape(N_GROUPS, SSM_STATE)
    return dlb_re, dlb_im, dbb_re, dbb_im, dc_re, dc_im


def _as_pieces(a):
    return a.reshape(N_DEV, a.shape[0] // N_DEV, a.shape[1])


def _hybrid_fwd(xn, x, wts, p, weights, riders):
    sv = {}
    hq = _mm(xn, wts['w_qkv_t'], "nt", "l0_in_qkv")
    gate = _mm(xn, wts['w_gate_t'], "nt", "l0_in_gate")
    ba = _mm(xn, wts['w_ba_t'], "nt", "l0_in_ba")
    u = _mm(xn, wts['w_u_t'], "nt", "l0_in_u")
    conv = p['conv_qkv']
    qkv = _qkv_pre_fwd(hq, conv, "l0_qkv_pre")
    gates = _gates_fwd(ba, p['arow'], p['brow'], "l0_gates")
    o, tm_all, s_all = riders.run("l0_gdr_fwd", _gdr_fwd, qkv, gates)
    wts['w_glu'], wts['w_out'] = weights.full['w_glu'], weights.full['w_out']
    y_a = _onorm_fwd(o, gate, p['onorm_g'], "l0_onorm")
    bu = riders.run("l0_s5_bu", _mm_bd, u, p['b_in'], "nn")
    xs = riders.run("l0_s5_scan", _s5_scan_fwd, bu, p['a_row'])
    weights.gather_by_sequencer(GATHER_LAYER1, xs, "gather_layer1", GATHER_LAYER1_ID)
    yc = riders.run("l0_s5_cx", _mm_bd, xs, p['c_out'], "nn")
    yl, y_b = _glu_fwd(yc, u, p['d_row'], wts['w_glu'], p['b_glu'], "l0_glu")
    mixed = jnp.concatenate([y_a, y_b], axis=1)
    x1 = _mm(mixed, wts['w_out'], "nn", "l0_out", res=x)
    sv.update(hq=hq, gate=gate, ba=ba, u=u, qkv=qkv, gb=gates, o=o, tm=tm_all, s=s_all, xs=xs, yl=yl, mixed=mixed)
    return x1, sv


def _hybrid_bwd(dx1, xn, wts, p, sv, riders):
    gr = {}
    dmixed = _mm(dx1, wts['w_out'], "nt", "l0_out_dx", out_dtype=BF16)
    riders.grad('w_out', _as_pieces(_mm(sv['mixed'], dx1, "tn", "l0_out_dw", out_dtype=BF16)))
    dya, dyb = dmixed[:, :WIDTH_A], dmixed[:, WIDTH_A:]
    dyl, du_direct, dw_glu, gr['b_glu_b'], dd = _glu_bwd(
        sv['yl'], sv['u'], p['d_row'], wts['w_glu'], p['b_glu'], dyb, "l0_glu_bwd")
    riders.grad('w_glu', dw_glu.astype(BF16).reshape(N_DEV, -1, PACK_COLS))
    dxs = riders.run("l0_s5_cx_dx", _mm_bd, dyl, p['c_out'], "nt")
    dc_out = _mm_bd(sv['xs'], dyl, "tn", "l0_s5_cx_dw")
    lam, da_row = riders.run("l0_s5_scan_bwd", _s5_scan_bwd, dxs, sv['xs'], p['a_row'])
    du = _mm_bd(lam, p['b_in'], "nt", "l0_s5_bu_dx", res=du_direct, out_dtype=BF16)
    db_in = _mm_bd(sv['u'], lam, "tn", "l0_s5_bu_dw")
    gr['s5'] = (db_in, dc_out, da_row, dd)
    do, dgate, gr['onorm_g_a'] = _onorm_bwd(sv['o'], sv['gate'], p['onorm_g'], dya, "l0_onorm_bwd")
    dqkv, dgb = riders.run("l0_gdr_bwd", _gdr_bwd, sv['qkv'], sv['gb'], sv['tm'], sv['s'], do)
    dhq, gr['conv_qkv_a'] = _qkv_pre_bwd(sv['hq'], p['conv_qkv'], dqkv, "l0_qkv_pre_bwd")
    dba, da_log, ddt_bias = _gates_bwd(sv['ba'], p['arow'], p['brow'], dgb, "l0_gates_bwd")
    gr['a_log_a'], gr['dt_bias_a'] = da_log[:, 4:8], ddt_bias[:, 4:8]
    dw_qkv_t = _mm(dhq, xn, "tn", "l0_in_qkv_dw", out_dtype=BF16)
    dw_gate_t = _mm(dgate, xn, "tn", "l0_in_gate_dw", out_dtype=BF16)
    dw_ba_t = _mm(dba, xn, "tn", "l0_in_ba_dw", out_dtype=BF16)
    dw_u_t = _mm(du, xn, "tn", "l0_in_u_dw", out_dtype=BF16)
    dw_in_t = _as_pieces(jnp.concatenate([dw_qkv_t, dw_gate_t, dw_ba_t[:8], dw_u_t], axis=0))
    riders.grad('w_in_t', jnp.concatenate(
        [dw_in_t, jnp.zeros((N_DEV, dict(PIECES)['w_in_t'] - W_IN_PIECE, D_MODEL), BF16)], axis=1))
    dxn = riders.run("l0_in_qkv_dx", _mm, dhq, wts['w_qkv_t'], "nn")
    dxn = _mm(dgate, wts['w_gate_t'], "nn", "l0_in_gate_dx", res=dxn)
    dxn = _mm(dba, wts['w_ba_t'], "nn", "l0_in_ba_dx", res=dxn)
    dxn = riders.run("l0_in_u_dx", _mm, du, wts['w_u_t'], "nn", res=dxn)
    return dxn, gr


def _xa_fwd(x1, g, mem_n, wq, wkv_t, wo, tag, riders):
    xq = _rms_fwd(x1, g, BF16, tag + "_norm")
    q = _mm(xq, wq, "nn", tag + "_q", out_dtype=BF16)
    kv = _mm(mem_n, wkv_t, "nt", tag + "_kv", out_dtype=BF16)
    o = riders.run(tag + "_attn", _attn_fwd, q, kv)
    x2 = _mm(o, wo, "nn", tag + "_o", res=x1)
    return x2, dict(xq=xq, q=q, kv=kv, o=o)


def _xa_bwd(dx2, x1, g, mem_n, wq, wkv_t, wo, sv, tag, layer, riders):
    do = _mm(dx2, wo, "nt", tag + "_o_dx", out_dtype=BF16)
    riders.grad('wo%d' % layer, _as_pieces(_mm(sv['o'], dx2, "tn", tag + "_o_dw", out_dtype=BF16)))
    dq, dk, dv = _attn_bwd(sv['q'], sv['kv'], do, tag + "_attn_bwd")
    dkv = jnp.concatenate([dk, dv], axis=1).astype(BF16)
    dxq = _mm(dq, wq, "nt", tag + "_q_dx")
    riders.grad('wq%d' % layer, _as_pieces(_mm(sv['xq'], dq, "tn", tag + "_q_dw", out_dtype=BF16)))
    dmem_n = _mm(dkv, wkv_t, "nn", tag + "_kv_dx")
    riders.grad('wkv_t%d' % layer, _as_pieces(_mm(dkv, mem_n, "tn", tag + "_kv_dw", out_dtype=BF16)))
    dx1, dg = riders.run(tag + "_norm_bwd", _rms_bwd, x1, g, dxq, dx2)
    return dx1, dmem_n, dg


def _ffn_fwd(x2, g, w_up_t, conv, w_down, tag, riders):
    xf = _rms_fwd(x2, g, BF16, tag + "_norm")
    h = riders.run(tag + "_up", _mm, xf, w_up_t, "nt")
    a = riders.run(tag + "_act", _ffn_act_fwd, h, conv)
    x3 = _mm(a, w_down, "nn", tag + "_down", res=x2)
    return x3, dict(xf=xf, h=h, a=a)


def _ffn_bwd(dx3, x2, g, w_up_t, conv, w_down, sv, tag, layer, riders):
    da = _mm(dx3, w_down, "nt", tag + "_down_dx")
    riders.grad('down%d' % layer, _as_pieces(_mm(sv['a'], dx3, "tn", tag + "_down_dw", out_dtype=BF16)))
    dh, dconv = riders.run(tag + "_act_bwd", _ffn_act_bwd, sv['h'], conv, da)
    dxf = riders.run(tag + "_up_dx", _mm_parts, dh, w_up_t, "nn")
    dw_up_t = riders.run(tag + "_up_dw", _mm_parts, dh, sv['xf'], "tn", out_dtype=BF16)
    riders.grad('up_t%d' % layer, _as_pieces(dw_up_t))
    dx2, dg = riders.run(tag + "_norm_bwd", _rms_bwd, x2, g, dxf, dx3)
    return dx2, dconv, dg


BIG_NAMES, SMALL_NAMES, REP_NAMES = list(BIG_SHARDED), list(SMALL_SHARDED), list(REPLICATED)
SMALL_SIZES = [int(np.prod(_shard_shape(*SMALL_SHARDED[n]))) for n in SMALL_NAMES]


PIECES = [('w_in_t', 384), ('w_glu', 32), ('w_out', 128), ('pool_w', 32), ('wq0', 128), ('wq1', 128),
          ('wkv_t0', 256), ('wkv_t1', 256), ('wo0', 128), ('wo1', 128), ('up_t0', 704), ('up_t1', 704),
          ('down0', 352), ('down1', 352)]
W_IN_ROWS = 4 * WIDTH_A + 2 * N_HEADS_A + SSM_WIDTH
W_IN_PIECE = W_IN_ROWS // N_DEV


def _row_tile(rows):
    return max(t for t in range(16, min(rows, 512) + 1, 16) if rows % t == 0)


class _Riders:
    def __init__(self):
        self.waiting = {}
        self.deferred = {}
        self.grads = {}
        self.groups = []
        self.reduced = {}

    def add(self, host, comm, then):
        self.waiting.setdefault(host, []).append((comm, then))

    def after(self, marker, then):
        self.deferred.setdefault(marker, []).append(then)

    def mark(self, name, out=None):
        for cont in self.deferred.pop(name, []):
            step = cont()
            if step is not None:
                values, then = step
                out, values = lax.optimization_barrier((out, values))
                then(values)
        return out

    def run(self, name, fn, *args, **kw):
        riders = self.waiting.pop(name, [])
        if not riders:
            out = fn(*args, name=name, **kw)
        else:
            out, couts = fn(*args, name=name, comm=[c for c, _ in riders], **kw)
            for (_, then), got in zip(riders, couts):
                then(got)
        return self.mark(name, out)

    def grad(self, key, pieces):
        self.grads[key] = pieces
        for group in [g for g in self.groups if all(k in self.grads for k in g[1])]:
            self.groups.remove(group)
            self._reduce(*group)

    def _reduce(self, name, keys, pair_marker, sum_marker):
        arrays = [self.grads[k] for k in keys]
        rows = sum(a.shape[1] for a in arrays)
        tile = _row_tile(rows)
        from_sibling = _sequencer_swap(arrays, name + "_to_sibling")

        def after_swap():
            if len(arrays) == 1:
                chip_sums = _pair_sum_pieces(arrays[0], from_sibling, name + "_pair_sum", tr=tile)
            else:
                core = lax.axis_index("c")
                keep = jnp.concatenate(
                    [lax.dynamic_index_in_dim(a.reshape(4, 2, a.shape[1], PACK_COLS), core, 1, keepdims=False)
                     for a in arrays], axis=1)
                chip_sums = _pair_sum(keep, from_sibling, name + "_pair_sum", tr=tile)

            def exchange_among_chips(chip_sums):
                from_chips = _sequencer_chips(chip_sums, name + "_to_chips")

                def store(total):
                    off = 0
                    for k, a in zip(keys, arrays):
                        self.reduced[k] = total[off:off + a.shape[1]]
                        off += a.shape[1]

                self.after(sum_marker, lambda: (
                    _chip_sum(chip_sums, from_chips, name + "_chip_sum", tr=tile), store))

            return chip_sums, exchange_among_chips

        self.after(pair_marker, after_swap)


class _Weights:
    def __init__(self, inp):
        bf = lambda a: a.astype(BF16)
        local = {'w_in_t': bf(inp['w_in_ab'][0]).T, 'w_glu': bf(inp['w_glu_b'][0]), 'w_out': bf(inp['w_out_ab'][0]),
                 'pool_w': bf(inp['pool_w'][0]),
                 'small': _pack([inp[n] for n in SMALL_NAMES])}
        for l in range(2):
            local['wq%d' % l] = bf(inp['xa_wq'][l])
            local['wkv_t%d' % l] = bf(inp['xa_wkv'][l]).T
            local['wo%d' % l] = bf(inp['xa_wo'][l])
            local['up_t%d' % l] = bf(inp['ffn_w_up'][l]).T
            local['down%d' % l] = bf(inp['ffn_w_down'][l])
        self.local, self.full = local, {}

    def plan(self, keys):
        return _gather_comm([self.local[k] for k in keys])

    def gather_by_sequencer(self, keys, after, name, collective_id):
        arrays = [self.local[k] for k in keys]
        tie = (after.reshape(-1)[0] * 0.0).astype(arrays[0].dtype)
        arrays[0] = arrays[0] + tie
        self.land(keys, _sequencer_gather(arrays, name, collective_id))

    def land(self, keys, gathered):
        for k, g in zip(keys, gathered):
            if k == 'small':
                off = 0
                for n, size in zip(SMALL_NAMES, SMALL_SIZES):
                    self.full[n] = _merge_shards(g.reshape(N_DEV, -1)[:, off:off + size], *SMALL_SHARDED[n])
                    off += size
            elif k == 'pool_w':
                self.full[k] = jnp.swapaxes(g, 0, 1).reshape(len(POOL_WINDOWS), POOL_GROUP, POOL_GROUP)
            else:
                self.full[k] = g.reshape(N_DEV * g.shape[1], g.shape[2])


GATHER_FIRST = ['w_in_t', 'small']
GATHER_LAYER0 = ['w_glu', 'w_out', 'wq0', 'wkv_t0', 'wo0', 'down0', 'up_t0']
GATHER_LAYER1 = ['pool_w', 'wq1', 'wkv_t1', 'wo1', 'up_t1', 'down1']
GATHER_LAYER0_ID, GATHER_LAYER1_ID = 7, 8
GRAD_RIDES = [('g_down1', ['down1'], 'l1_ffn_up_dx', 'l1_xa_norm_bwd'),
              ('g_up1', ['up_t1'], 'l1_xa_norm_bwd', 'l0_ffn_up_dx'),
              ('g_xa1', ['wq1', 'wkv_t1', 'wo1', 'pool_w'], 'l0_ffn_up_dx', 'l0_xa_norm_bwd'),
              ('g_down0', ['down0'], 'l0_ffn_up_dx', 'l0_s5_scan_bwd'),
              ('g_l0', ['up_t0', 'wq0', 'wkv_t0', 'wo0'], 'l0_s5_cx_dx', 'l0_in_u_dx'),
              ('g_out', ['w_out', 'w_glu'], 'l0_s5_scan_bwd', 'l0_in_u_dx'),
              ('g_in', ['w_in_t'], 'l0_in_u_dx', 'adamw_pool_w')]


def _local_step(inp):
    f32_of = lambda n: inp[n].astype(F32)
    weights = _Weights(inp)
    riders = _Riders()
    riders.groups = list(GRAD_RIDES)
    full = weights.full
    weights.land(GATHER_FIRST, _comm_only(weights.plan(GATHER_FIRST), "gather_first"))
    weights.gather_by_sequencer(GATHER_LAYER0, full['w_in_t'], "gather_layer0", GATHER_LAYER0_ID)
    w_in_t = full['w_in_t']
    wts0 = dict(w_qkv_t=w_in_t[:3 * WIDTH_A], w_gate_t=w_in_t[3 * WIDTH_A:4 * WIDTH_A],
                w_ba_t=jnp.concatenate([w_in_t[4 * WIDTH_A:4 * WIDTH_A + 8], jnp.zeros((LANE - 8, D_MODEL), BF16)], 0),
                w_u_t=w_in_t[4 * WIDTH_A + 8:])
    lb_disc, disc_vjp = jax.vjp(_s5_discretise, f32_of('ssm_lambda_re')[0], f32_of('ssm_lambda_im')[0],
                                f32_of('ssm_b_re')[0], f32_of('ssm_b_im')[0], f32_of('ssm_log_dt')[0])
    b_in, c_out, a_row = _s5_matrices(*lb_disc, f32_of('ssm_c_re')[0], f32_of('ssm_c_im')[0])
    zeros4 = jnp.zeros((1, 4), F32)
    p0 = dict(conv_qkv=full['conv_qkv_a'][0], onorm_g=f32_of('onorm_g_a'),
              arow=jnp.concatenate([zeros4, f32_of('a_log_a'), jnp.zeros((1, LANE - 8), F32)], 1),
              brow=jnp.concatenate([zeros4, f32_of('dt_bias_a'), jnp.zeros((1, LANE - 8), F32)], 1),
              b_in=b_in.astype(BF16), c_out=c_out.astype(BF16), a_row=a_row,
              d_row=f32_of('ssm_d').reshape(1, SSM_WIDTH), b_glu=f32_of('b_glu_b'))

    x0 = inp['x'][0]
    mem_n = _rms_fwd(inp['mem'][0], inp['norm_mem_g'], BF16, "mem_norm")
    xn0 = _rms_fwd(x0, inp['norm_mix_g'][0], BF16, "l0_mix_norm")
    x1, sv_mix0 = _hybrid_fwd(xn0, x0, wts0, p0, weights, riders)
    x2, sv_xa0 = _xa_fwd(x1, inp['norm_xa_g'][0], mem_n, full['wq0'], full['wkv_t0'], full['wo0'], "l0_xa", riders)
    x3, sv_ffn0 = _ffn_fwd(x2, inp['norm_ffn_g'][0], full['up_t0'], full['ffn_conv'][0], full['down0'], "l0_ffn", riders)
    xn1 = _rms_fwd(x3, inp['norm_mix_g'][1], F32, "l1_mix_norm")
    x4 = _pool_fwd(xn1, full['pool_w'], full['pool_scale'], x3, "l1_pool")
    x5, sv_xa1 = _xa_fwd(x4, inp['norm_xa_g'][1], mem_n, full['wq1'], full['wkv_t1'], full['wo1'], "l1_xa", riders)
    x6, sv_ffn1 = _ffn_fwd(x5, inp['norm_ffn_g'][1], full['up_t1'], full['ffn_conv'][1], full['down1'], "l1_ffn", riders)
    loss_part, dx6, dg_final = _loss_head(x6, inp['norm_final_g'], inp['loss_target'][0], "loss_head")

    dx5, dconv1, dg_ffn1 = _ffn_bwd(dx6, x5, inp['norm_ffn_g'][1], full['up_t1'], full['ffn_conv'][1], full['down1'],
                                    sv_ffn1, "l1_ffn", 1, riders)
    dx4, dmem1, dg_xa1 = _xa_bwd(dx5, x4, inp['norm_xa_g'][1], mem_n, full['wq1'], full['wkv_t1'], full['wo1'],
                                 sv_xa1, "l1_xa", 1, riders)
    dxn1, dpool_w, dpool_scale = _pool_bwd(xn1, full['pool_w'], full['pool_scale'], dx4, "l1_pool_bwd")
    pool_pieces = jnp.swapaxes(dpool_w.astype(BF16).reshape(len(POOL_WINDOWS), N_DEV, -1, POOL_GROUP), 0, 1)
    riders.grad('pool_w', pool_pieces.reshape(N_DEV, -1, PACK_COLS))
    dx3, dg_mix1 = riders.run("l1_mix_norm_bwd", _rms_bwd, x3, inp['norm_mix_g'][1], dxn1, dx4)
    dx2, dconv0, dg_ffn0 = _ffn_bwd(dx3, x2, inp['norm_ffn_g'][0], full['up_t0'], full['ffn_conv'][0], full['down0'],
                                    sv_ffn0, "l0_ffn", 0, riders)
    dx1, dmem0, dg_xa0 = _xa_bwd(dx2, x1, inp['norm_xa_g'][0], mem_n, full['wq0'], full['wkv_t0'], full['wo0'],
                                 sv_xa0, "l0_xa", 0, riders)
    dxn0, g_mix0 = _hybrid_bwd(dx1, xn0, wts0, p0, sv_mix0, riders)
    grad_x, dg_mix0 = _rms_bwd(x0, inp['norm_mix_g'][0], dxn0, dx1, "l0_mix_norm_bwd")
    _, dg_mem = _rms_bwd(inp['mem'][0], inp['norm_mem_g'], dmem0 + dmem1, None, "mem_norm_bwd")
    assert not riders.groups and not riders.waiting and all(k.startswith("adamw_") for k in riders.deferred), (
        riders.groups, list(riders.waiting), list(riders.deferred))

    db_in, dc_out, da_row, dd = g_mix0['s5']
    dlb_re, dlb_im, dbb_re, dbb_im, dc_re, dc_im = _s5_matrix_grads(db_in, dc_out, da_row)
    dlam_re, dlam_im, dbr, dbi, dlog_dt = disc_vjp((dlb_re, dlb_im, dbb_re, dbb_im))

    rep_grads = {
        'norm_mix_g': jnp.concatenate([dg_mix0, dg_mix1], 0), 'norm_xa_g': jnp.concatenate([dg_xa0, dg_xa1], 0),
        'norm_ffn_g': jnp.concatenate([dg_ffn0, dg_ffn1], 0), 'norm_mem_g': dg_mem.reshape(-1),
        'norm_final_g': dg_final.reshape(-1), 'a_log_a': g_mix0['a_log_a'], 'dt_bias_a': g_mix0['dt_bias_a'],
        'onorm_g_a': g_mix0['onorm_g_a'], 'ssm_lambda_re': dlam_re[None], 'ssm_lambda_im': dlam_im[None],
        'ssm_b_re': dbr[None], 'ssm_b_im': dbi[None], 'ssm_c_re': dc_re[None], 'ssm_c_im': dc_im[None],
        'ssm_d': dd.reshape(1, N_GROUPS, SSM_GROUP), 'ssm_log_dt': dlog_dt[None], 'b_glu_b': g_mix0['b_glu_b']}
    small_grads = {'conv_qkv_a': g_mix0['conv_qkv_a'][None], 'pool_scale': dpool_scale,
                   'ffn_conv': jnp.stack([dconv0, dconv1])}
    return loss_part, grad_x, riders, rep_grads, small_grads


ADAMW_ORDER = ['ffn_w_up', 'ffn_w_down', 'xa_wkv', 'xa_wq', 'xa_wo', 'w_out_ab', 'w_glu_b', 'pool_w', 'w_in_ab']


def _update(inp, loss_part, grad_x, riders, rep_grads, small_grads):
    dev = _device_index()
    misc_local = _pack([rep_grads[n] for n in REP_NAMES] + [small_grads[n] for n in SMALL_NAMES] + [loss_part])
    (misc_all,) = _sequencer_gather([misc_local], "gather_small_grads", GATHER_LAYER0_ID)
    piece = lambda key: riders.reduced[key]
    both = lambda name: jnp.stack([piece(name + '0'), piece(name + '1')])
    swap = lambda a: jnp.swapaxes(a, -1, -2)
    reduced = {'w_in_ab': lambda: piece('w_in_t')[:W_IN_PIECE][None],
               'w_glu_b': lambda: piece('w_glu').reshape(inp['w_glu_b'].shape),
               'w_out_ab': lambda: piece('w_out')[None], 'pool_w': lambda: piece('pool_w').reshape(inp['pool_w'].shape),
               'xa_wq': lambda: both('wq'), 'xa_wkv': lambda: both('wkv_t'), 'xa_wo': lambda: both('wo'),
               'ffn_w_up': lambda: both('up_t'), 'ffn_w_down': lambda: both('down')}
    transposed = ('w_in_ab', 'xa_wkv', 'ffn_w_up')
    grads, upd = {}, {}
    assert sorted(ADAMW_ORDER) == sorted(BIG_NAMES)
    for n in ADAMW_ORDER:
        fix = swap if n in transposed else (lambda a: a)
        g = reduced[n]()
        out = riders.run("adamw_" + n, _adamw, fix(inp[n]), g, fix(inp['m_' + n]), fix(inp['v_' + n]))
        upd[n], grads[n] = tuple(fix(o) for o in out), fix(g)
    assert not riders.waiting and not riders.deferred, (list(riders.waiting), list(riders.deferred))
    misc_sum = _sum_leading(misc_all, "small_grads_sum")
    misc = _unpack(misc_sum, [inp[n].shape for n in REP_NAMES] + [SMALL_SHARDED[n][0] for n in SMALL_NAMES] + [()])
    loss = misc.pop()
    for n, g in zip(REP_NAMES, misc):
        grads[n] = g
    for n, g in zip(SMALL_NAMES, misc[len(REP_NAMES):]):
        grads[n] = lax.dynamic_index_in_dim(_split_shards(g, SMALL_SHARDED[n][1]), dev, 0, keepdims=False
                                            ).reshape(inp[n].shape)
    tiny_names = REP_NAMES + SMALL_NAMES
    rep_total = sum(int(np.prod(inp[n].shape)) for n in REP_NAMES)
    packs = [_pack([inp[prefix + n] for n in tiny_names]) for prefix in ('', 'm_', 'v_')]
    g_pack = _pack([misc_sum.reshape(-1)[:rep_total]] + [grads[n] for n in SMALL_NAMES])
    tiny_out = [_unpack(o, [inp[n].shape for n in tiny_names])
                for o in _adamw(packs[0], g_pack, packs[1], packs[2], "adamw_small")]
    for i, n in enumerate(tiny_names):
        upd[n] = tuple(o[i] for o in tiny_out)

    outs = [loss, grad_x[None]]
    outs += [grads[n] for n in WEIGHT_NAMES]
    for i in range(3):
        outs += [upd[n][i] for n in WEIGHT_NAMES]
    return tuple(outs)


def _step(inp):
    loss_part, grad_x, riders, rep_grads, small_grads = _local_step(inp)
    return _update(inp, loss_part, grad_x, riders, rep_grads, small_grads)


INPUT_NAMES = (['x', 'mem'] + WEIGHT_NAMES + ['loss_target'] + ['m_' + n for n in WEIGHT_NAMES]
               + ['v_' + n for n in WEIGHT_NAMES])


def kernel(x, mem, norm_mix_g, norm_xa_g, norm_ffn_g, norm_mem_g, norm_final_g, w_in_ab, conv_qkv_a, a_log_a, dt_bias_a, onorm_g_a, ssm_lambda_re, ssm_lambda_im, ssm_b_re, ssm_b_im, ssm_c_re, ssm_c_im, ssm_d, ssm_log_dt, w_glu_b, b_glu_b, w_out_ab, pool_w, pool_scale, xa_wq, xa_wkv, xa_wo, ffn_w_up, ffn_conv, ffn_w_down, loss_target, m_norm_mix_g, m_norm_xa_g, m_norm_ffn_g, m_norm_mem_g, m_norm_final_g, m_w_in_ab, m_conv_qkv_a, m_a_log_a, m_dt_bias_a, m_onorm_g_a, m_ssm_lambda_re, m_ssm_lambda_im, m_ssm_b_re, m_ssm_b_im, m_ssm_c_re, m_ssm_c_im, m_ssm_d, m_ssm_log_dt, m_w_glu_b, m_b_glu_b, m_w_out_ab, m_pool_w, m_pool_scale, m_xa_wq, m_xa_wkv, m_xa_wo, m_ffn_w_up, m_ffn_conv, m_ffn_w_down, v_norm_mix_g, v_norm_xa_g, v_norm_ffn_g, v_norm_mem_g, v_norm_final_g, v_w_in_ab, v_conv_qkv_a, v_a_log_a, v_dt_bias_a, v_onorm_g_a, v_ssm_lambda_re, v_ssm_lambda_im, v_ssm_b_re, v_ssm_b_im, v_ssm_c_re, v_ssm_c_im, v_ssm_d, v_ssm_log_dt, v_w_glu_b, v_b_glu_b, v_w_out_ab, v_pool_w, v_pool_scale, v_xa_wq, v_xa_wkv, v_xa_wo, v_ffn_w_up, v_ffn_conv, v_ffn_w_down):
    args = (x, mem, norm_mix_g, norm_xa_g, norm_ffn_g, norm_mem_g, norm_final_g, w_in_ab, conv_qkv_a, a_log_a, dt_bias_a, onorm_g_a, ssm_lambda_re, ssm_lambda_im, ssm_b_re, ssm_b_im, ssm_c_re, ssm_c_im, ssm_d, ssm_log_dt, w_glu_b, b_glu_b, w_out_ab, pool_w, pool_scale, xa_wq, xa_wkv, xa_wo, ffn_w_up, ffn_conv, ffn_w_down, loss_target, m_norm_mix_g, m_norm_xa_g, m_norm_ffn_g, m_norm_mem_g, m_norm_final_g, m_w_in_ab, m_conv_qkv_a, m_a_log_a, m_dt_bias_a, m_onorm_g_a, m_ssm_lambda_re, m_ssm_lambda_im, m_ssm_b_re, m_ssm_b_im, m_ssm_c_re, m_ssm_c_im, m_ssm_d, m_ssm_log_dt, m_w_glu_b, m_b_glu_b, m_w_out_ab, m_pool_w, m_pool_scale, m_xa_wq, m_xa_wkv, m_xa_wo, m_ffn_w_up, m_ffn_conv, m_ffn_w_down, v_norm_mix_g, v_norm_xa_g, v_norm_ffn_g, v_norm_mem_g, v_norm_final_g, v_w_in_ab, v_conv_qkv_a, v_a_log_a, v_dt_bias_a, v_onorm_g_a, v_ssm_lambda_re, v_ssm_lambda_im, v_ssm_b_re, v_ssm_b_im, v_ssm_c_re, v_ssm_c_im, v_ssm_d, v_ssm_log_dt, v_w_glu_b, v_b_glu_b, v_w_out_ab, v_pool_w, v_pool_scale, v_xa_wq, v_xa_wkv, v_xa_wo, v_ffn_w_up, v_ffn_conv, v_ffn_w_down)
    return _step(dict(zip(INPUT_NAMES, args)))
```

```python
import functools
import math

import numpy as np
import jax
import jax.numpy as jnp
from jax import lax
from jax.experimental import pallas as pl
from jax.experimental.pallas import tpu as pltpu
from jax.experimental.pallas import tpu_sc as plsc

F32, BF16 = jnp.float32, jnp.bfloat16
HIGH, HIGHEST = lax.Precision.HIGH, lax.Precision.HIGHEST
MESH = pl.DeviceIdType.MESH

N_DEV = 8
SEQ, D_MODEL, MEM_LEN = 2048, 1024, 256
WIDTH_A, N_HEADS_A, HEAD_A, CONV_A = 512, 4, 128, 4
GDR_CHUNK = 128
SSM_WIDTH, SSM_GROUP, N_GROUPS, SSM_STATE = 512, 16, 32, 64
SSM_CH = N_GROUPS * SSM_STATE
SCAN_CB = 512
POOL_WINDOWS = (2, 4, 8, 16)
POOL_GROUP = 256
N_HEADS_X, HEAD_X = 4, 256
D_FF, CONV_FFN = 2816, 3
RMS_EPS = 1e-6
ADAM_LR, ADAM_B1, ADAM_B2, ADAM_EPS, ADAM_WD, ADAM_STEP = 0.001, 0.9, 0.999, 1e-08, 0.01, 10
LANE = 128
PACK_COLS = 1024
VMEM_LIMIT_BYTES = 56 * 1024 * 1024


def _params(sem=None):
    return pltpu.CompilerParams(dimension_semantics=sem, vmem_limit_bytes=VMEM_LIMIT_BYTES)


class Comm:
    def __init__(self, inputs, out_shapes, sems, start, end, mid=None):
        self.inputs, self.out_shapes, self.sems = list(inputs), list(out_shapes), list(sems)
        self.start, self.mid, self.end = start, mid, end


def _merge_comms(comms):
    comms = [c for c in comms if c is not None]
    if not comms:
        return None, []
    bounds, ni, no, ns = [], 0, 0, 0
    for c in comms:
        bounds.append((ni, no, ns))
        ni, no, ns = ni + len(c.inputs), no + len(c.out_shapes), ns + len(c.sems)

    def phase(which):
        def run(ins, outs, sems):
            for c, (i0, o0, s0) in zip(comms, bounds):
                fn = getattr(c, which)
                if fn is not None:
                    fn(ins[i0:i0 + len(c.inputs)], outs[o0:o0 + len(c.out_shapes)], sems[s0:s0 + len(c.sems)])
        return run

    merged = Comm([a for c in comms for a in c.inputs], [s for c in comms for s in c.out_shapes],
                  [s for c in comms for s in c.sems], phase("start"), phase("end"), phase("mid"))
    return merged, [(o0, o0 + len(c.out_shapes)) for c, (_, o0, _) in zip(comms, bounds)]


def _call(body, *, name, grid, in_specs, out_specs, out_shape, args, scratch_shapes=(), sem=None, comm=None):
    single = not isinstance(out_shape, (list, tuple))
    out_specs_l = [out_specs] if single else list(out_specs)
    out_shape_l = [out_shape] if single else list(out_shape)
    scratch_shapes = list(scratch_shapes)
    merged, spans = _merge_comms(comm if isinstance(comm, (list, tuple)) else [comm])
    if merged is None:
        outs = pl.pallas_call(body, name=name, grid=grid, in_specs=list(in_specs), out_specs=out_specs_l,
                              out_shape=out_shape_l, scratch_shapes=scratch_shapes, compiler_params=_params(sem))(*args)
        outs = outs[0] if single else outs
        return outs if comm is None else (outs, [])
    n_in, n_out, n_scr = len(in_specs), len(out_specs_l), len(scratch_shapes)
    ci, co = len(merged.inputs), len(merged.out_shapes)
    total = int(np.prod(grid))

    def wrapped(*refs):
        ins, cins = refs[:n_in], refs[n_in:n_in + ci]
        outs, couts = refs[n_in + ci:n_in + ci + n_out], refs[n_in + ci + n_out:n_in + ci + n_out + co]
        scr, csems = refs[n_in + ci + n_out + co:n_in + ci + n_out + co + n_scr], refs[n_in + ci + n_out + co + n_scr:]
        lin = pl.program_id(0)
        for d in range(1, len(grid)):
            lin = lin * grid[d] + pl.program_id(d)
        pl.when(lin == 0)(lambda: merged.start(cins, couts, csems))
        body(*ins, *outs, *scr)
        mid_step = min((3 * total) // 4, total - 1)
        pl.when(lin == mid_step)(lambda: merged.mid(cins, couts, csems))
        pl.when(lin == total - 1)(lambda: merged.end(cins, couts, csems))

    any_spec = pl.BlockSpec(memory_space=pl.ANY)
    res = pl.pallas_call(
        wrapped, name=name, grid=grid, in_specs=list(in_specs) + [any_spec] * ci,
        out_specs=out_specs_l + [any_spec] * co, out_shape=out_shape_l + merged.out_shapes,
        scratch_shapes=scratch_shapes + merged.sems,
        compiler_params=_params(("arbitrary",) * len(grid)))(*args, *merged.inputs)
    outs, couts = res[:n_out], res[n_out:]
    return (outs[0] if single else list(outs)), [list(couts[a:b]) for a, b in spans]


def _comm_only(comm, name):
    def body():
        pass

    _, couts = _call(body, name=name, grid=(1,), in_specs=[], out_specs=[], out_shape=[], args=[], comm=comm)
    return couts[0]


def _tile(dim, pref):
    best = None
    for t in range(LANE, min(dim, pref) + 1, LANE):
        if dim % t == 0:
            best = t
    return best if best is not None else dim


MM_VMEM_BUDGET = 40 * 1024 * 1024


def _mm_tiles(m, n, k, a_bytes, b_bytes, o_bytes, r_bytes):
    for tk in (k, _tile(k, 2048), _tile(k, 1024), _tile(k, 512)):
        for tm, tn in ((1024, 1536), (1024, 1024), (1024, 512), (512, 512), (256, 512), (256, 256)):
            tm, tn = _tile(m, tm), _tile(n, tn)
            acc = 0 if tk == k else tm * tn * 4
            need = 2 * (tm * tk * a_bytes + tk * tn * b_bytes + tm * tn * (o_bytes + r_bytes)) + acc
            if need <= MM_VMEM_BUDGET:
                return tm, tn, tk
    raise ValueError("no matmul tiling fits VMEM")


def _mm(a, b, mode, name, out_dtype=F32, res=None, comm=None):
    if mode == "nn":
        (m, k), n = a.shape, b.shape[1]
    elif mode == "nt":
        (m, k), n = a.shape, b.shape[0]
    else:
        (k, m), n = a.shape, b.shape[1]
    tm, tn, tk = _mm_tiles(m, n, k, a.dtype.itemsize, b.dtype.itemsize, jnp.dtype(out_dtype).itemsize,
                           0 if res is None else res.dtype.itemsize)
    nk = k // tk
    dims = {"nn": ((1,), (0,)), "nt": ((1,), (1,)), "tn": ((0,), (0,))}[mode]

    def body(*refs):
        if res is None:
            a_ref, b_ref, o_ref = refs[:3]
            r_ref = None
        else:
            a_ref, b_ref, r_ref, o_ref = refs[:4]
        part = lax.dot_general(a_ref[...].astype(BF16), b_ref[...].astype(BF16), (dims, ((), ())),
                               preferred_element_type=F32)

        def finish(out):
            if r_ref is not None:
                out = out + r_ref[...].astype(F32)
            o_ref[...] = out.astype(out_dtype)

        if nk == 1:
            finish(part)
            return
        acc = refs[-1]
        kk = pl.program_id(2)

        @pl.when(kk == 0)
        def _():
            acc[...] = part

        @pl.when(kk > 0)
        def _():
            acc[...] += part

        @pl.when(kk == nk - 1)
        def _():
            finish(acc[...])

    a_spec = (pl.BlockSpec((tk, tm), lambda i, j, q: (q, i)) if mode == "tn"
              else pl.BlockSpec((tm, tk), lambda i, j, q: (i, q)))
    b_spec = (pl.BlockSpec((tn, tk), lambda i, j, q: (j, q)) if mode == "nt"
              else pl.BlockSpec((tk, tn), lambda i, j, q: (q, j)))
    o_spec = pl.BlockSpec((tm, tn), lambda i, j, q: (i, j))
    in_specs, args = [a_spec, b_spec], [a, b]
    if res is not None:
        in_specs.append(o_spec)
        args.append(res)
    return _call(body, name=name, grid=(m // tm, n // tn, nk), in_specs=in_specs, out_specs=o_spec,
                 out_shape=jax.ShapeDtypeStruct((m, n), out_dtype),
                 scratch_shapes=[] if nk == 1 else [pltpu.VMEM((tm, tn), F32)],
                 sem=("parallel", "parallel", "arbitrary"), args=args, comm=comm)


def _mm_parts(parts, b, mode, name, out_dtype=F32):
    count = len(parts)
    rows, cols = parts[0].shape
    n = b.shape[1]
    if mode == "nn":
        tm, tn, tk = _mm_tiles(rows, n, count * cols, parts[0].dtype.itemsize, b.dtype.itemsize,
                               jnp.dtype(out_dtype).itemsize, 0)
        assert tk == count * cols

        def body(*refs):
            a_refs, b_refs, o_ref = refs[:count], refs[count:2 * count], refs[2 * count]
            out = None
            for a_ref, b_ref in zip(a_refs, b_refs):
                part = jnp.dot(a_ref[...].astype(BF16), b_ref[...].astype(BF16), preferred_element_type=F32)
                out = part if out is None else out + part
            o_ref[...] = out.astype(out_dtype)

        return _call(body, name=name, grid=(rows // tm, n // tn),
                     in_specs=[pl.BlockSpec((tm, cols), lambda i, j: (i, 0))] * count
                     + [pl.BlockSpec((cols, tn), functools.partial(lambda q, i, j: (q, j), q)) for q in range(count)],
                     out_specs=pl.BlockSpec((tm, tn), lambda i, j: (i, j)),
                     out_shape=jax.ShapeDtypeStruct((rows, n), out_dtype),
                     sem=("parallel", "parallel"), args=(*parts, *([b] * count)))
    tm, tn = _tile(cols, 1536), _tile(n, 1024)
    per = cols // tm

    def body_t(*refs):
        a_refs, b_ref, o_ref = refs[:count], refs[count], refs[count + 1]
        for q, a_ref in enumerate(a_refs):
            @pl.when(pl.program_id(0) // per == q)
            def _(a_ref=a_ref):
                o_ref[...] = lax.dot_general(a_ref[...].astype(BF16), b_ref[...].astype(BF16),
                                             (((0,), (0,)), ((), ())), preferred_element_type=F32).astype(out_dtype)

    return _call(body_t, name=name, grid=(count * per, n // tn),
                 in_specs=[pl.BlockSpec((rows, tm), functools.partial(lambda q, i, j: (0, jnp.clip(i - q * per, 0, per - 1)), q))
                           for q in range(count)] + [pl.BlockSpec((rows, tn), lambda i, j: (0, j))],
                 out_specs=pl.BlockSpec((tm, tn), lambda i, j: (i, j)),
                 out_shape=jax.ShapeDtypeStruct((count * cols, n), out_dtype),
                 sem=("parallel", "parallel"), args=(*parts, b))


def _mm_bd(a, b, mode, name, out_dtype=F32, res=None, comm=None, tm=1024):
    if mode == "tn":
        k = a.shape[0]
        nb = min(a.shape[1], b.shape[1]) // LANE
        ma, n = a.shape[1] // nb, b.shape[1] // nb

        def body(a_ref, b_ref, o_ref):
            o_ref[0] = lax.dot_general(a_ref[...].astype(BF16), b_ref[...].astype(BF16), (((0,), (0,)), ((), ())),
                                       preferred_element_type=F32).astype(out_dtype)

        return _call(body, name=name, grid=(nb,),
                     in_specs=[pl.BlockSpec((k, ma), lambda j: (0, j)), pl.BlockSpec((k, n), lambda j: (0, j))],
                     out_specs=pl.BlockSpec((1, ma, n), lambda j: (j, 0, 0)),
                     out_shape=jax.ShapeDtypeStruct((nb, ma, n), out_dtype), sem=("parallel",), args=(a, b), comm=comm)
    m = a.shape[0]
    nb = b.shape[0]
    ka = a.shape[1] // nb
    n = b.shape[2] if mode == "nn" else b.shape[1]
    tm = _tile(m, tm)
    dims = ((1,), (0,)) if mode == "nn" else ((1,), (1,))

    def body(*refs):
        if res is None:
            a_ref, b_ref, o_ref = refs
            r_ref = None
        else:
            a_ref, b_ref, r_ref, o_ref = refs
        out = lax.dot_general(a_ref[...].astype(BF16), b_ref[0].astype(BF16), (dims, ((), ())),
                              preferred_element_type=F32)
        if r_ref is not None:
            out = out + r_ref[...].astype(F32)
        o_ref[...] = out.astype(out_dtype)

    o_spec = pl.BlockSpec((tm, n), lambda i, j: (i, j))
    in_specs = [pl.BlockSpec((tm, ka), lambda i, j: (i, j)), pl.BlockSpec((1,) + b.shape[1:], lambda i, j: (j, 0, 0))]
    args = [a, b]
    if res is not None:
        in_specs.append(o_spec)
        args.append(res)
    return _call(body, name=name, grid=(m // tm, nb), in_specs=in_specs, out_specs=o_spec,
                 out_shape=jax.ShapeDtypeStruct((m, nb * n), out_dtype), sem=("parallel", "parallel"),
                 args=args, comm=comm)


def _rms_fwd(x, g, out_dtype, name, tr=256):
    rows, d = x.shape

    def body(x_ref, g_ref, o_ref):
        xv = x_ref[...]
        r = lax.rsqrt(jnp.mean(xv * xv, axis=-1, keepdims=True) + RMS_EPS)
        o_ref[...] = (xv * r * g_ref[...]).astype(out_dtype)

    return pl.pallas_call(
        body, name=name, grid=(rows // tr,),
        in_specs=[pl.BlockSpec((tr, d), lambda i: (i, 0)), pl.BlockSpec((1, d), lambda i: (0, 0))],
        out_specs=pl.BlockSpec((tr, d), lambda i: (i, 0)), out_shape=jax.ShapeDtypeStruct((rows, d), out_dtype),
        compiler_params=_params(("parallel",)))(x, g.reshape(1, d))


def _rms_bwd(x, g, dy, dres, name, tr=256, comm=None):
    rows, d = x.shape

    def body(*refs):
        if dres is None:
            x_ref, g_ref, dy_ref, dx_ref, dg_ref = refs
            r_ref = None
        else:
            x_ref, g_ref, dy_ref, r_ref, dx_ref, dg_ref = refs

        @pl.when(pl.program_id(0) == 0)
        def _():
            dg_ref[...] = jnp.zeros_like(dg_ref)

        xv, dyv = x_ref[...], dy_ref[...].astype(F32)
        r = lax.rsqrt(jnp.mean(xv * xv, axis=-1, keepdims=True) + RMS_EPS)
        xh = xv * r
        dyg = dyv * g_ref[...]
        dx = r * (dyg - xh * jnp.mean(dyg * xh, axis=-1, keepdims=True))
        if r_ref is not None:
            dx = dx + r_ref[...]
        dx_ref[...] = dx
        dg_ref[...] += jnp.sum(dyv * xh, axis=0, keepdims=True)

    blk = pl.BlockSpec((tr, d), lambda i: (i, 0))
    vec = pl.BlockSpec((1, d), lambda i: (0, 0))
    in_specs, args = [blk, vec, blk], [x, g.reshape(1, d), dy]
    if dres is not None:
        in_specs.append(blk)
        args.append(dres)
    return _call(
        body, name=name, grid=(rows // tr,), in_specs=in_specs, out_specs=[blk, vec],
        out_shape=[jax.ShapeDtypeStruct((rows, d), F32), jax.ShapeDtypeStruct((1, d), F32)],
        sem=("arbitrary",), args=args, comm=comm)


def _loss_head(x, g, target, name, tr=256):
    rows, d = x.shape

    def body(x_ref, g_ref, t_ref, loss_ref, dx_ref, dg_ref):
        @pl.when(pl.program_id(0) == 0)
        def _():
            dg_ref[...] = jnp.zeros_like(dg_ref)
            loss_ref[...] = jnp.zeros_like(loss_ref)

        xv = x_ref[...]
        r = lax.rsqrt(jnp.mean(xv * xv, axis=-1, keepdims=True) + RMS_EPS)
        xh = xv * r
        err = xh * g_ref[...] - t_ref[...]
        loss_ref[...] += 0.5 * jnp.sum(jnp.mean(err * err, axis=-1, keepdims=True), keepdims=True)
        dyv = err * (1.0 / d)
        dyg = dyv * g_ref[...]
        dx_ref[...] = r * (dyg - xh * jnp.mean(dyg * xh, axis=-1, keepdims=True))
        dg_ref[...] += jnp.sum(dyv * xh, axis=0, keepdims=True)

    blk = pl.BlockSpec((tr, d), lambda i: (i, 0))
    vec = pl.BlockSpec((1, d), lambda i: (0, 0))
    return pl.pallas_call(
        body, name=name, grid=(rows // tr,), in_specs=[blk, vec, blk],
        out_specs=[pl.BlockSpec((1, 1), lambda i: (0, 0)), blk, vec],
        out_shape=[jax.ShapeDtypeStruct((1, 1), F32), jax.ShapeDtypeStruct((rows, d), F32),
                   jax.ShapeDtypeStruct((1, d), F32)],
        compiler_params=_params(("arbitrary",)))(x, g.reshape(1, d), target)


def _shift_down(x, s):
    rows = lax.broadcasted_iota(jnp.int32, x.shape, 0)
    return jnp.where(rows >= s, pltpu.roll(x, s, 0), 0.0)


def _shift_up(x, s):
    n = x.shape[0]
    rows = lax.broadcasted_iota(jnp.int32, x.shape, 0)
    return jnp.where(rows < n - s, pltpu.roll(x, n - s, 0), 0.0)


def _sigmoid(x):
    return 1.0 / (1.0 + jnp.exp(-x))


def _silu_and_grad(x):
    s = _sigmoid(x)
    return x * s, s * (1.0 + x * (1.0 - s))


_GELU_C0, _GELU_C1 = math.sqrt(2.0 / math.pi), 0.044715


def _gelu_and_grad(x):
    th = jnp.tanh(_GELU_C0 * (x + _GELU_C1 * x * x * x))
    y = 0.5 * x * (1.0 + th)
    dy = 0.5 * (1.0 + th) + 0.5 * x * (1.0 - th * th) * _GELU_C0 * (1.0 + 3.0 * _GELU_C1 * x * x)
    return y, dy


def _ffn_act_fwd(h, w, name, tc=256, comm=None):
    t = h.shape[0]
    nb = D_FF // tc

    def body(hg_ref, hv_ref, wg_ref, wv_ref, a_ref):
        def conv(x, wr):
            return wr[2:3, :] * x + wr[1:2, :] * _shift_down(x, 1) + wr[0:1, :] * _shift_down(x, 2)

        cg = conv(hg_ref[...], wg_ref[...])
        cv = conv(hv_ref[...], wv_ref[...])
        a_ref[...] = (cg * _sigmoid(cg) * cv).astype(BF16)

    return _call(
        body, name=name, grid=(nb,),
        in_specs=[pl.BlockSpec((t, tc), lambda j: (0, j)), pl.BlockSpec((t, tc), lambda j: (0, j + nb)),
                  pl.BlockSpec((CONV_FFN, tc), lambda j: (0, j)), pl.BlockSpec((CONV_FFN, tc), lambda j: (0, j + nb))],
        out_specs=pl.BlockSpec((t, tc), lambda j: (0, j)), out_shape=jax.ShapeDtypeStruct((t, D_FF), BF16),
        sem=("parallel",), args=(h, h, w, w), comm=comm)


def _ffn_act_bwd(h, w, da, name, tc=256, comm=None):
    t = h.shape[0]
    nb = D_FF // tc

    def body(hg_ref, hv_ref, wg_ref, wv_ref, da_ref, dhg_ref, dhv_ref, dwg_ref, dwv_ref):
        hg, hv, wg, wv = hg_ref[...], hv_ref[...], wg_ref[...], wv_ref[...]
        hg1, hg2, hv1, hv2 = _shift_down(hg, 1), _shift_down(hg, 2), _shift_down(hv, 1), _shift_down(hv, 2)
        cg = wg[2:3, :] * hg + wg[1:2, :] * hg1 + wg[0:1, :] * hg2
        cv = wv[2:3, :] * hv + wv[1:2, :] * hv1 + wv[0:1, :] * hv2
        sg, dsg = _silu_and_grad(cg)
        dav = da_ref[...].astype(F32)
        dcv = dav * sg
        dcg = dav * cv * dsg

        def conv_t(dc, wr):
            return wr[2:3, :] * dc + wr[1:2, :] * _shift_up(dc, 1) + wr[0:1, :] * _shift_up(dc, 2)

        dhg_ref[...] = conv_t(dcg, wg).astype(BF16)
        dhv_ref[...] = conv_t(dcv, wv).astype(BF16)
        dwg_ref[0:1, :] = jnp.sum(dcg * hg2, axis=0, keepdims=True)
        dwg_ref[1:2, :] = jnp.sum(dcg * hg1, axis=0, keepdims=True)
        dwg_ref[2:3, :] = jnp.sum(dcg * hg, axis=0, keepdims=True)
        dwv_ref[0:1, :] = jnp.sum(dcv * hv2, axis=0, keepdims=True)
        dwv_ref[1:2, :] = jnp.sum(dcv * hv1, axis=0, keepdims=True)
        dwv_ref[2:3, :] = jnp.sum(dcv * hv, axis=0, keepdims=True)

    big = lambda off: pl.BlockSpec((t, tc), lambda j: (0, j + off))
    small = lambda off: pl.BlockSpec((CONV_FFN, tc), lambda j: (0, j + off))
    res = _call(
        body, name=name, grid=(nb,),
        in_specs=[big(0), big(nb), small(0), small(nb), big(0)],
        out_specs=[big(0), big(0), small(0), small(0)],
        out_shape=[jax.ShapeDtypeStruct((t, D_FF), BF16), jax.ShapeDtypeStruct((t, D_FF), BF16),
                   jax.ShapeDtypeStruct((CONV_FFN, D_FF), F32), jax.ShapeDtypeStruct((CONV_FFN, D_FF), F32)],
        sem=("parallel",), args=(h, h, w, w, da), comm=comm)
    (dhg, dhv, dwg, dwv), couts = res if comm is not None else (res, None)
    out = ((dhg, dhv), jnp.concatenate([dwg, dwv], axis=1))
    return out if comm is None else (out, couts)


def _attn_probs(q, k):
    s = lax.dot_general(q.astype(BF16), k.astype(BF16), (((1,), (1,)), ((), ())),
                        preferred_element_type=F32) * (HEAD_X ** -0.5)
    s = s - jnp.max(s, axis=-1, keepdims=True)
    p = jnp.exp(s)
    return p / jnp.sum(p, axis=-1, keepdims=True)


def _attn_fwd(q, kv, name, tq=512, comm=None):
    t = q.shape[0]

    def body(q_ref, k_ref, v_ref, o_ref):
        p = _attn_probs(q_ref[...], k_ref[...])
        o_ref[...] = jnp.dot(p.astype(BF16), v_ref[...].astype(BF16), preferred_element_type=F32).astype(BF16)

    return _call(
        body, name=name, grid=(N_HEADS_X, t // tq),
        in_specs=[pl.BlockSpec((tq, HEAD_X), lambda h, i: (i, h)),
                  pl.BlockSpec((MEM_LEN, HEAD_X), lambda h, i: (0, h)),
                  pl.BlockSpec((MEM_LEN, HEAD_X), lambda h, i: (0, h + N_HEADS_X))],
        out_specs=pl.BlockSpec((tq, HEAD_X), lambda h, i: (i, h)),
        out_shape=jax.ShapeDtypeStruct((t, N_HEADS_X * HEAD_X), BF16),
        sem=("parallel", "parallel"), args=(q, kv, kv), comm=comm)


def _attn_bwd(q, kv, do, name, tq=512):
    t = q.shape[0]

    def body(q_ref, k_ref, v_ref, do_ref, dq_ref, dk_ref, dv_ref):
        @pl.when(pl.program_id(1) == 0)
        def _():
            dk_ref[...] = jnp.zeros_like(dk_ref)
            dv_ref[...] = jnp.zeros_like(dv_ref)

        qb, kb, vb, dob = (r[...].astype(BF16) for r in (q_ref, k_ref, v_ref, do_ref))
        p = _attn_probs(qb, kb)
        dp = lax.dot_general(dob, vb, (((1,), (1,)), ((), ())), preferred_element_type=F32)
        ds = p * (dp - jnp.sum(dp * p, axis=-1, keepdims=True)) * (HEAD_X ** -0.5)
        dsb = ds.astype(BF16)
        dq_ref[...] = jnp.dot(dsb, kb, preferred_element_type=F32).astype(BF16)
        dk_ref[...] += lax.dot_general(dsb, qb, (((0,), (0,)), ((), ())), preferred_element_type=F32)
        dv_ref[...] += lax.dot_general(p.astype(BF16), dob, (((0,), (0,)), ((), ())), preferred_element_type=F32)

    qs = pl.BlockSpec((tq, HEAD_X), lambda h, i: (i, h))
    ms = pl.BlockSpec((MEM_LEN, HEAD_X), lambda h, i: (0, h))
    return pl.pallas_call(
        body, name=name, grid=(N_HEADS_X, t // tq),
        in_specs=[qs, ms, pl.BlockSpec((MEM_LEN, HEAD_X), lambda h, i: (0, h + N_HEADS_X)), qs],
        out_specs=[qs, ms, ms],
        out_shape=[jax.ShapeDtypeStruct((t, D_MODEL), BF16), jax.ShapeDtypeStruct((MEM_LEN, D_MODEL), F32),
                   jax.ShapeDtypeStruct((MEM_LEN, D_MODEL), F32)],
        compiler_params=_params(("parallel", "arbitrary")))(q, kv, kv, do)


def _pool_counts(t, win):
    pos = lax.broadcasted_iota(jnp.int32, (t, 1), 0).astype(F32) + 1.0
    return 1.0 / jnp.minimum(pos, float(win))


def _pool_delta(xv, win):
    s, step = xv, 1
    while step < win:
        s = s + _shift_down(s, step)
        step *= 2
    return s * _pool_counts(xv.shape[0], win) - xv


def _pool_delta_t(dv, win):
    s, step = dv * _pool_counts(dv.shape[0], win), 1
    while step < win:
        s = s + _shift_up(s, step)
        step *= 2
    return s - dv


def _pool_fwd(xn, w, scale, res, name):
    t = xn.shape[0]

    def make_branch(win, xn_ref, w_ref, s_ref, r_ref, o_ref):
        def branch():
            dl = _pool_delta(xn_ref[...], win)
            y = jnp.dot(dl.astype(BF16), w_ref[0], preferred_element_type=F32)
            o_ref[...] = r_ref[...] + y * s_ref[...]
        return branch

    def body(xn_ref, w_ref, s_ref, r_ref, o_ref):
        for gi, win in enumerate(POOL_WINDOWS):
            pl.when(pl.program_id(0) == gi)(make_branch(win, xn_ref, w_ref, s_ref, r_ref, o_ref))

    blk = pl.BlockSpec((t, POOL_GROUP), lambda g: (0, g))
    return pl.pallas_call(
        body, name=name, grid=(len(POOL_WINDOWS),),
        in_specs=[blk, pl.BlockSpec((1, POOL_GROUP, POOL_GROUP), lambda g: (g, 0, 0)),
                  pl.BlockSpec((1, POOL_GROUP), lambda g: (0, g)), blk],
        out_specs=blk, out_shape=jax.ShapeDtypeStruct((t, D_MODEL), F32),
        compiler_params=_params(("parallel",)))(xn, w, scale, res)


def _pool_bwd(xn, w, scale, dmix, name):
    t = xn.shape[0]

    def make_branch(win, xn_ref, w_ref, s_ref, d_ref, dxn_ref, dw_ref, ds_ref):
        def branch():
            dl = _pool_delta(xn_ref[...], win).astype(BF16)
            wv = w_ref[0]
            dm = d_ref[...]
            y = jnp.dot(dl, wv, preferred_element_type=F32)
            ds_ref[...] = jnp.sum(dm * y, axis=0, keepdims=True)
            dy = (dm * s_ref[...]).astype(BF16)
            dw_ref[0] = lax.dot_general(dl, dy, (((0,), (0,)), ((), ())), preferred_element_type=F32)
            ddl = lax.dot_general(dy, wv, (((1,), (1,)), ((), ())), preferred_element_type=F32)
            dxn_ref[...] = _pool_delta_t(ddl, win)
        return branch

    def body(*refs):
        for gi, win in enumerate(POOL_WINDOWS):
            pl.when(pl.program_id(0) == gi)(make_branch(win, *refs))

    blk = pl.BlockSpec((t, POOL_GROUP), lambda g: (0, g))
    wspec = pl.BlockSpec((1, POOL_GROUP, POOL_GROUP), lambda g: (g, 0, 0))
    vec = pl.BlockSpec((1, POOL_GROUP), lambda g: (0, g))
    return pl.pallas_call(
        body, name=name, grid=(len(POOL_WINDOWS),), in_specs=[blk, wspec, vec, blk], out_specs=[blk, wspec, vec],
        out_shape=[jax.ShapeDtypeStruct((t, D_MODEL), F32),
                   jax.ShapeDtypeStruct((len(POOL_WINDOWS), POOL_GROUP, POOL_GROUP), F32),
                   jax.ShapeDtypeStruct((1, D_MODEL), F32)],
        compiler_params=_params(("parallel",)))(xn, w, scale, dmix)


def _qkv_conv(h, wr):
    return (wr[3:4, :] * h + wr[2:3, :] * _shift_down(h, 1) + wr[1:2, :] * _shift_down(h, 2)
            + wr[0:1, :] * _shift_down(h, 3))


def _qkv_block_kind(j):
    return j < 2 * N_HEADS_A, jnp.where(j < N_HEADS_A, HEAD_A ** -0.5, 1.0)


def _qkv_pre_fwd(h, w, name):
    t, cols = h.shape

    def body(h_ref, w_ref, o_ref):
        normalised, scale = _qkv_block_kind(pl.program_id(0))
        c = _qkv_conv(h_ref[...], w_ref[...])
        s = c * _sigmoid(c)
        r = lax.rsqrt(jnp.sum(s * s, axis=-1, keepdims=True) + 1e-6)
        o_ref[...] = jnp.where(normalised, s * (r * scale), s)

    blk = pl.BlockSpec((t, HEAD_A), lambda j: (0, j))
    return pl.pallas_call(
        body, name=name, grid=(cols // HEAD_A,), in_specs=[blk, pl.BlockSpec((CONV_A, HEAD_A), lambda j: (0, j))],
        out_specs=blk, out_shape=jax.ShapeDtypeStruct((t, cols), F32), compiler_params=_params(("parallel",)))(h, w)


def _qkv_pre_bwd(h, w, dy, name):
    t, cols = h.shape

    def body(h_ref, w_ref, dy_ref, dh_ref, dw_ref):
        normalised, scale = _qkv_block_kind(pl.program_id(0))
        hv, wr, dyv = h_ref[...], w_ref[...], dy_ref[...]
        h1, h2, h3 = _shift_down(hv, 1), _shift_down(hv, 2), _shift_down(hv, 3)
        c = wr[3:4, :] * hv + wr[2:3, :] * h1 + wr[1:2, :] * h2 + wr[0:1, :] * h3
        s, dsilu = _silu_and_grad(c)
        r = lax.rsqrt(jnp.sum(s * s, axis=-1, keepdims=True) + 1e-6)
        y = s * r
        dys = dyv * scale
        ds = jnp.where(normalised, r * (dys - y * jnp.sum(dys * y, axis=-1, keepdims=True)), dyv)
        dc = ds * dsilu
        dh = (wr[3:4, :] * dc + wr[2:3, :] * _shift_up(dc, 1) + wr[1:2, :] * _shift_up(dc, 2)
              + wr[0:1, :] * _shift_up(dc, 3))
        dh_ref[...] = dh.astype(BF16)
        dw_ref[0:1, :] = jnp.sum(dc * h3, axis=0, keepdims=True)
        dw_ref[1:2, :] = jnp.sum(dc * h2, axis=0, keepdims=True)
        dw_ref[2:3, :] = jnp.sum(dc * h1, axis=0, keepdims=True)
        dw_ref[3:4, :] = jnp.sum(dc * hv, axis=0, keepdims=True)

    blk = pl.BlockSpec((t, HEAD_A), lambda j: (0, j))
    taps = pl.BlockSpec((CONV_A, HEAD_A), lambda j: (0, j))
    return pl.pallas_call(
        body, name=name, grid=(cols // HEAD_A,), in_specs=[blk, taps, blk], out_specs=[blk, taps],
        out_shape=[jax.ShapeDtypeStruct((t, cols), BF16), jax.ShapeDtypeStruct((CONV_A, cols), F32)],
        compiler_params=_params(("parallel",)))(h, w, dy)


def _softplus(x):
    return jnp.maximum(x, 0.0) + jnp.log1p(jnp.exp(-jnp.abs(x)))


def _gates_fwd(ba, arow, brow, name):
    t = ba.shape[0]

    def body(x_ref, a_ref, b_ref, o_ref):
        xv = x_ref[...]
        lane = lax.broadcasted_iota(jnp.int32, xv.shape, 1)
        beta = _sigmoid(xv)
        g = -jnp.exp(a_ref[...]) * _softplus(xv + b_ref[...])
        o_ref[...] = jnp.where(lane < N_HEADS_A, beta, jnp.where(lane < 2 * N_HEADS_A, g, 0.0))

    return pl.pallas_call(body, name=name, out_shape=jax.ShapeDtypeStruct((t, LANE), F32),
                          compiler_params=_params())(ba, arow, brow)


def _gates_bwd(ba, arow, brow, dgb, name):
    t = ba.shape[0]

    def body(x_ref, a_ref, b_ref, d_ref, dx_ref, da_ref, db_ref):
        xv = x_ref[...]
        dv = d_ref[0] + d_ref[1] + d_ref[2] + d_ref[3]
        lane = lax.broadcasted_iota(jnp.int32, xv.shape, 1)
        beta = _sigmoid(xv)
        ea = jnp.exp(a_ref[...])
        z = xv + b_ref[...]
        dgv = jnp.where((lane >= N_HEADS_A) & (lane < 2 * N_HEADS_A), dv, 0.0) * (-ea)
        dz = dgv * _sigmoid(z)
        dx = jnp.where(lane < N_HEADS_A, dv * beta * (1.0 - beta), dz)
        dx_ref[...] = dx.astype(BF16)
        db_ref[...] = jnp.sum(dz, axis=0, keepdims=True)
        da_ref[...] = jnp.sum(dgv * _softplus(z), axis=0, keepdims=True)

    return pl.pallas_call(
        body, name=name,
        out_shape=[jax.ShapeDtypeStruct((t, LANE), BF16), jax.ShapeDtypeStruct((1, LANE), F32),
                   jax.ShapeDtypeStruct((1, LANE), F32)],
        compiler_params=_params())(ba, arow, brow, dgb)


def _head_gates(gates, head):
    lane = lax.broadcasted_iota(jnp.int32, gates.shape, 1)
    beta = jnp.sum(jnp.where(lane == head, gates, 0.0), axis=1, keepdims=True)
    g = jnp.sum(jnp.where(lane == head + N_HEADS_A, gates, 0.0), axis=1, keepdims=True)
    return beta, g


_B_NN, _B_NT, _B_TN = ((2,), (1,)), ((2,), (2,)), ((1,), (1,))


def _bdot(a, b, dims=_B_NN, prec=None):
    if prec is None:
        a, b = a.astype(BF16), b.astype(BF16)
    return lax.dot_general(a, b, (dims, ((0,), (0,))), precision=prec, preferred_element_type=F32)


def _heads_of(ref):
    return jnp.stack([ref[:, h * HEAD_A:(h + 1) * HEAD_A] for h in range(N_HEADS_A)])


def _all_head_gates(gates):
    pairs = [_head_gates(gates, h) for h in range(N_HEADS_A)]
    return jnp.stack([b for b, _ in pairs]), jnp.stack([g for _, g in pairs])


def _gdr_terms(k, beta, g):
    h, c = k.shape[0], GDR_CHUNK
    row = lax.broadcasted_iota(jnp.int32, (c, c), 0)
    col = lax.broadcasted_iota(jnp.int32, (c, c), 1)
    causal, strict = row >= col, row > col
    lower = jnp.broadcast_to(causal.astype(F32), (h, c, c))
    gcum = _bdot(lower, jnp.broadcast_to(g, (h, c, c)), prec=HIGHEST)
    diff = gcum - jnp.swapaxes(gcum, 1, 2)
    decay = jnp.where(causal, jnp.exp(jnp.where(causal, diff, 0.0)), 0.0)
    kb = k * beta
    return row, col, causal, strict, gcum, decay, kb, _bdot(kb, k, _B_NT)


def _unit_lower_inverses(a):
    c = a.shape[1]
    eye = (lax.broadcasted_iota(jnp.int32, (c, c), 0) == lax.broadcasted_iota(jnp.int32, (c, c), 1)).astype(F32)
    p = -a
    inv = eye + p
    step = 1
    while 2 * step < c:
        p = _bdot(p, p, prec=HIGH)
        inv = inv + _bdot(inv, p, prec=HIGH)
        step *= 2
    return inv


def _gdr_fwd(qkv, gates, name, comm=None):
    t = qkv.shape[0]
    c, nh = GDR_CHUNK, N_HEADS_A
    n = t // c

    def body(q_ref, k_ref, v_ref, gb_ref, o_ref, tm_ref, s_ref, state):
        @pl.when(pl.program_id(0) == 0)
        def _():
            state[...] = jnp.zeros_like(state)

        qv, kv, vv = _heads_of(q_ref), _heads_of(k_ref), _heads_of(v_ref)
        beta, g = _all_head_gates(gb_ref[...])
        row, col, causal, strict, gcum, decay, kb, kk = _gdr_terms(kv, beta, g)
        tm = _unit_lower_inverses(jnp.where(strict, kk * decay, 0.0))
        e = jnp.exp(gcum)
        u = _bdot(tm, vv * beta, prec=HIGH)
        w = _bdot(tm, kb * e, prec=HIGH)
        p = jnp.where(causal, _bdot(qv, kv, _B_NT) * decay, 0.0)
        s = state[...]
        s_ref[:, 0] = s
        tm_ref[:, 0] = tm
        vn = u - _bdot(w, s)
        o = _bdot(qv * e, s) + _bdot(p, vn)
        for h in range(nh):
            o_ref[:, h * HEAD_A:(h + 1) * HEAD_A] = o[h]
        glast = gcum[:, c - 1:c, :]
        state[...] = s * jnp.exp(glast) + _bdot(kv * jnp.exp(glast - gcum), vn, _B_TN)

    part = lambda p: pl.BlockSpec((c, WIDTH_A), lambda i: (i, p))
    mat = pl.BlockSpec((nh, 1, c, c), lambda i: (0, i, 0, 0))
    return _call(
        body, name=name, grid=(n,), in_specs=[part(0), part(1), part(2), pl.BlockSpec((c, LANE), lambda i: (i, 0))],
        out_specs=[part(0), mat, mat],
        out_shape=[jax.ShapeDtypeStruct((t, WIDTH_A), F32), jax.ShapeDtypeStruct((nh, n, c, c), F32),
                   jax.ShapeDtypeStruct((nh, n, HEAD_A, HEAD_A), F32)],
        scratch_shapes=[pltpu.VMEM((nh, HEAD_A, HEAD_A), F32)], sem=("arbitrary",),
        args=(qkv, qkv, qkv, gates), comm=comm)


def _gdr_bwd(qkv, gates, tm_all, s_all, do, name, comm=None):
    t = qkv.shape[0]
    c, nh = GDR_CHUNK, N_HEADS_A
    n = t // c

    def body(q_ref, k_ref, v_ref, gb_ref, tm_ref, s_ref, do_ref, dqkv_ref, dgb_ref, dstate):
        @pl.when(pl.program_id(0) == 0)
        def _():
            dstate[...] = jnp.zeros_like(dstate)

        qv, kv, vv, dov = _heads_of(q_ref), _heads_of(k_ref), _heads_of(v_ref), _heads_of(do_ref)
        beta, g = _all_head_gates(gb_ref[...])
        tm, s, dsp = tm_ref[:, 0], s_ref[:, 0], dstate[...]
        row, col, causal, strict, gcum, decay, kb, kk = _gdr_terms(kv, beta, g)
        rowsum = lambda x: jnp.sum(x, axis=2, keepdims=True)
        e = jnp.exp(gcum)
        vb, kbe = vv * beta, kb * e
        u = _bdot(tm, vb, prec=HIGH)
        w = _bdot(tm, kbe, prec=HIGH)
        qk = _bdot(qv, kv, _B_NT)
        p = jnp.where(causal, qk * decay, 0.0)
        vn = u - _bdot(w, s)
        glast = gcum[:, c - 1:c, :]
        el = jnp.exp(glast)
        f = jnp.exp(glast - gcum)
        kd = kv * f
        qe = qv * e

        dvn = _bdot(p, dov, _B_TN) + _bdot(kd, dsp)
        dglast = el[:, :, 0:1] * jnp.sum(s * dsp, axis=(1, 2), keepdims=True)
        dkd = _bdot(vn, dsp, _B_NT)
        dk = dkd * f
        df = rowsum(dkd * kv) * f[:, :, 0:1]
        dglast = dglast + jnp.sum(df, axis=1, keepdims=True)
        dgc = -df
        dp = jnp.where(causal, _bdot(dov, vn, _B_NT), 0.0)
        dqe = _bdot(dov, s, _B_NT)
        dq = dqe * e
        de = rowsum(dqe * qv)
        dstate[...] = dsp * el + _bdot(qe, dov, _B_TN) - _bdot(w, dvn, _B_TN)
        dw = -_bdot(dvn, s, _B_NT)
        dvb = _bdot(tm, dvn, _B_TN, prec=HIGH)
        dkbe = _bdot(tm, dw, _B_TN, prec=HIGH)
        da = -jnp.where(strict, _bdot(dvb, u, _B_NT) + _bdot(dkbe, w, _B_NT), 0.0)
        dkk = da * decay
        dqk = dp * decay
        dd = da * kk + dp * qk
        dq = dq + _bdot(dqk, kv)
        dk = dk + _bdot(dqk, qv, _B_TN)
        dkb = _bdot(dkk, kv) + dkbe * e
        dk = dk + _bdot(dkk, kb, _B_TN)
        de = de + rowsum(dkbe * kb)
        dk = dk + dkb * beta
        dbeta = rowsum(dkb * kv) + rowsum(dvb * vv)
        m = dd * decay
        dgc = dgc + rowsum(m) - rowsum(jnp.swapaxes(m, 1, 2))
        dgc = dgc + de * e[:, :, 0:1]
        dgc = dgc + jnp.where(row[:, 0:1] == c - 1, dglast, 0.0)
        upper = jnp.broadcast_to((row <= col).astype(F32), (nh, c, c))
        dg = _bdot(upper, jnp.broadcast_to(dgc, (nh, c, c)), prec=HIGHEST)
        dv = dvb * beta
        for p, grad in enumerate((dq, dk, dv)):
            for h in range(nh):
                dqkv_ref[:, p * WIDTH_A + h * HEAD_A:p * WIDTH_A + (h + 1) * HEAD_A] = grad[h]
        head = lax.broadcasted_iota(jnp.int32, (nh, c, LANE), 0)
        lane = lax.broadcasted_iota(jnp.int32, (nh, c, LANE), 2)
        dgb_ref[...] = jnp.where(lane == head, dbeta, jnp.where(lane == head + nh, dg, 0.0))

    part = lambda p: pl.BlockSpec((c, WIDTH_A), lambda i: (n - 1 - i, p))
    mat = pl.BlockSpec((nh, 1, c, c), lambda i: (0, n - 1 - i, 0, 0))
    return _call(
        body, name=name, grid=(n,),
        in_specs=[part(0), part(1), part(2), pl.BlockSpec((c, LANE), lambda i: (n - 1 - i, 0)), mat, mat, part(0)],
        out_specs=[pl.BlockSpec((c, 3 * WIDTH_A), lambda i: (n - 1 - i, 0)),
                   pl.BlockSpec((nh, c, LANE), lambda i: (0, n - 1 - i, 0))],
        out_shape=[jax.ShapeDtypeStruct((t, 3 * WIDTH_A), F32), jax.ShapeDtypeStruct((nh, t, LANE), F32)],
        scratch_shapes=[pltpu.VMEM((nh, HEAD_A, HEAD_A), F32)], sem=("arbitrary",),
        args=(qkv, qkv, qkv, gates, tm_all, s_all, do), comm=comm)


def _onorm_fwd(o, gate, g, name):
    t = o.shape[0]

    def body(o_ref, gate_ref, g_ref, y_ref):
        ov, gv = o_ref[...], gate_ref[...]
        r = lax.rsqrt(jnp.mean(ov * ov, axis=-1, keepdims=True) + RMS_EPS)
        y_ref[...] = (ov * r * g_ref[...] * gv * _sigmoid(gv)).astype(BF16)

    blk = pl.BlockSpec((t, HEAD_A), lambda j: (0, j))
    return pl.pallas_call(
        body, name=name, grid=(N_HEADS_A,), in_specs=[blk, blk, pl.BlockSpec((1, HEAD_A), lambda j: (0, 0))],
        out_specs=blk, out_shape=jax.ShapeDtypeStruct((t, WIDTH_A), BF16),
        compiler_params=_params(("parallel",)))(o, gate, g)


def _onorm_bwd(o, gate, g, dy, name):
    t = o.shape[0]

    def body(o_ref, gate_ref, g_ref, dy_ref, do_ref, dgate_ref, dg_ref):
        @pl.when(pl.program_id(0) == 0)
        def _():
            dg_ref[...] = jnp.zeros_like(dg_ref)

        ov, gv, dyv = o_ref[...], gate_ref[...], dy_ref[...].astype(F32)
        r = lax.rsqrt(jnp.mean(ov * ov, axis=-1, keepdims=True) + RMS_EPS)
        oh = ov * r
        sg, dsg = _silu_and_grad(gv)
        dgate_ref[...] = (dyv * oh * g_ref[...] * dsg).astype(BF16)
        dn = dyv * sg
        dg_ref[...] += jnp.sum(dn * oh, axis=0, keepdims=True)
        dng = dn * g_ref[...]
        do_ref[...] = r * (dng - oh * jnp.mean(dng * oh, axis=-1, keepdims=True))

    blk = pl.BlockSpec((t, HEAD_A), lambda j: (0, j))
    vec = pl.BlockSpec((1, HEAD_A), lambda j: (0, 0))
    return pl.pallas_call(
        body, name=name, grid=(N_HEADS_A,), in_specs=[blk, blk, vec, blk], out_specs=[blk, blk, vec],
        out_shape=[jax.ShapeDtypeStruct((t, WIDTH_A), F32), jax.ShapeDtypeStruct((t, WIDTH_A), BF16),
                   jax.ShapeDtypeStruct((1, HEAD_A), F32)],
        compiler_params=_params(("arbitrary",)))(o, gate, g, dy)


def _cmul(ar, ai, br, bi):
    return ar * br - ai * bi, ar * bi + ai * br


def _scan_tables(ar, ai, reverse):
    p1 = (ar, ai)
    p2 = _cmul(*p1, *p1)
    p4 = _cmul(*p2, *p2)
    p8 = _cmul(*p4, *p4)
    p3 = _cmul(*p2, *p1)
    p5 = _cmul(*p4, *p1)
    p6 = _cmul(*p4, *p2)
    p7 = _cmul(*p4, *p3)
    pows = [p1, p2, p3, p4, p5, p6, p7, p8]
    rows = lax.broadcasted_iota(jnp.int32, (8, ar.shape[1]), 0)
    tr = jnp.zeros((8, ar.shape[1]), F32)
    ti = jnp.zeros((8, ar.shape[1]), F32)
    for r in range(8):
        pw = pows[7 - r] if reverse else pows[r]
        tr = jnp.where(rows == r, pw[0], tr)
        ti = jnp.where(rows == r, pw[1], ti)
    return p1, p2, p4, p8, tr, ti


def _tile_scan(xr, xi, p1, p2, p4, reverse):
    rows = lax.broadcasted_iota(jnp.int32, xr.shape, 0)
    for s, (pr, pi) in ((1, p1), (2, p2), (4, p4)):
        if reverse:
            keep = rows < 8 - s
            sr, si = pltpu.roll(xr, 8 - s, 0), pltpu.roll(xi, 8 - s, 0)
        else:
            keep = rows >= s
            sr, si = pltpu.roll(xr, s, 0), pltpu.roll(xi, s, 0)
        sr, si = jnp.where(keep, sr, 0.0), jnp.where(keep, si, 0.0)
        mr, mi = _cmul(pr, pi, sr, si)
        xr, xi = xr + mr, xi + mi
    return xr, xi


def _s5_scan_fwd(bu, a, name, tb=512, comm=None):
    t = bu.shape[0]
    cb = SCAN_CB
    nt = t // tb

    def body(b_ref, a_ref, x_ref, carry):
        @pl.when(pl.program_id(1) == 0)
        def _():
            carry[...] = jnp.zeros_like(carry)

        ar, ai = a_ref[:, 0:cb], a_ref[:, cb:2 * cb]
        p1, p2, p4, p8, tr, ti = _scan_tables(ar, ai, False)

        def step(j, c):
            cr, ci = c
            i = pl.multiple_of(j * 8, 8)
            xr, xi = _tile_scan(b_ref[pl.ds(i, 8), 0:cb], b_ref[pl.ds(i, 8), cb:2 * cb], p1, p2, p4, False)
            mr, mi = _cmul(tr, ti, cr, ci)
            xr, xi = xr + mr, xi + mi
            x_ref[pl.ds(i, 8), 0:cb] = xr
            x_ref[pl.ds(i, 8), cb:2 * cb] = xi
            return xr[7:8, :], xi[7:8, :]

        cr, ci = lax.fori_loop(0, tb // 8, step, (carry[0:1, :], carry[1:2, :]), unroll=2)
        carry[0:1, :] = cr
        carry[1:2, :] = ci

    blk = pl.BlockSpec((tb, 2 * cb), lambda j, i: (i, j))
    return _call(
        body, name=name, grid=(SSM_CH // cb, nt),
        in_specs=[blk, pl.BlockSpec((1, 2 * cb), lambda j, i: (0, j))], out_specs=blk,
        out_shape=jax.ShapeDtypeStruct((t, 2 * SSM_CH), F32), scratch_shapes=[pltpu.VMEM((8, cb), F32)],
        sem=("parallel", "arbitrary"), args=(bu, a), comm=comm)


def _s5_scan_bwd(dx, x, a, name, tb=512, comm=None):
    t = dx.shape[0]
    cb = SCAN_CB
    nt = t // tb
    nj = tb // 8

    def body(d_ref, x_ref, xp_ref, a_ref, l_ref, da_ref, carry, acc):
        tblk = pl.program_id(1)

        @pl.when(tblk == 0)
        def _():
            carry[...] = jnp.zeros_like(carry)
            acc[...] = jnp.zeros_like(acc)

        ar, ai = a_ref[:, 0:cb], a_ref[:, cb:2 * cb]
        p1, p2, p4, p8, tr, ti = _scan_tables(ar, -ai, True)
        rows = lax.broadcasted_iota(jnp.int32, (8, cb), 0)

        def step(jj, c):
            cr, ci, sr_acc, si_acc = c
            j = nj - 1 - jj
            i = pl.multiple_of(j * 8, 8)
            lr, li = _tile_scan(d_ref[pl.ds(i, 8), 0:cb], d_ref[pl.ds(i, 8), cb:2 * cb], p1, p2, p4, True)
            mr, mi = _cmul(tr, ti, cr, ci)
            lr, li = lr + mr, li + mi
            l_ref[pl.ds(i, 8), 0:cb] = lr
            l_ref[pl.ds(i, 8), cb:2 * cb] = li
            ip = pl.multiple_of(jnp.maximum(j - 1, 0) * 8, 8)
            prev_r = jnp.where(j > 0, x_ref[pl.ds(ip, 8), 0:cb], xp_ref[:, 0:cb])
            prev_i = jnp.where(j > 0, x_ref[pl.ds(ip, 8), cb:2 * cb], xp_ref[:, cb:2 * cb])
            edge = jnp.where(jnp.logical_and(j == 0, tblk == nt - 1), 0.0, 1.0)
            xs_r = jnp.where(rows == 0, pltpu.roll(prev_r, 1, 0) * edge, pltpu.roll(x_ref[pl.ds(i, 8), 0:cb], 1, 0))
            xs_i = jnp.where(rows == 0, pltpu.roll(prev_i, 1, 0) * edge, pltpu.roll(x_ref[pl.ds(i, 8), cb:2 * cb], 1, 0))
            sr_acc = sr_acc + lr * xs_r + li * xs_i
            si_acc = si_acc + li * xs_r - lr * xs_i
            return lr[0:1, :], li[0:1, :], sr_acc, si_acc

        cr, ci, sr_acc, si_acc = lax.fori_loop(
            0, nj, step, (carry[0:1, :], carry[1:2, :], acc[:, 0:cb], acc[:, cb:2 * cb]))
        carry[0:1, :] = cr
        carry[1:2, :] = ci
        acc[:, 0:cb] = sr_acc
        acc[:, cb:2 * cb] = si_acc

        @pl.when(tblk == nt - 1)
        def _():
            da_ref[...] = jnp.sum(acc[...], axis=0, keepdims=True)

    blk = pl.BlockSpec((tb, 2 * cb), lambda j, i: (nt - 1 - i, j))
    prev = pl.BlockSpec((8, 2 * cb), lambda j, i: (jnp.maximum((nt - 1 - i) * (tb // 8) - 1, 0), j))
    vec = pl.BlockSpec((1, 2 * cb), lambda j, i: (0, j))
    return _call(
        body, name=name, grid=(SSM_CH // cb, nt), in_specs=[blk, blk, prev, vec], out_specs=[blk, vec],
        out_shape=[jax.ShapeDtypeStruct((t, 2 * SSM_CH), F32), jax.ShapeDtypeStruct((1, 2 * SSM_CH), F32)],
        scratch_shapes=[pltpu.VMEM((8, cb), F32), pltpu.VMEM((8, 2 * cb), F32)],
        sem=("parallel", "arbitrary"), args=(dx, x, x, a), comm=comm)


def _glu_fwd(yc, u, dvec, wg, bg, name, tr=256):
    t = yc.shape[0]

    def body(yc_ref, u_ref, d_ref, w_ref, b_ref, yl_ref, yb_ref):
        yl = yc_ref[...] + d_ref[...] * u_ref[...]
        yl_ref[...] = yl
        yg, _ = _gelu_and_grad(yl)
        z = jnp.dot(yg.astype(BF16), w_ref[...], preferred_element_type=F32) + b_ref[...]
        yb_ref[...] = (yg * _sigmoid(z)).astype(BF16)

    blk = pl.BlockSpec((tr, SSM_WIDTH), lambda i: (i, 0))
    vec = pl.BlockSpec((1, SSM_WIDTH), lambda i: (0, 0))
    return pl.pallas_call(
        body, name=name, grid=(t // tr,),
        in_specs=[blk, blk, vec, pl.BlockSpec((SSM_WIDTH, SSM_WIDTH), lambda i: (0, 0)), vec],
        out_specs=[blk, blk],
        out_shape=[jax.ShapeDtypeStruct((t, SSM_WIDTH), F32), jax.ShapeDtypeStruct((t, SSM_WIDTH), BF16)],
        compiler_params=_params(("parallel",)))(yc, u, dvec, wg, bg)


def _glu_bwd(yl, u, dvec, wg, bg, dyb, name, tr=256):
    t = yl.shape[0]

    def body(yl_ref, u_ref, d_ref, w_ref, b_ref, dy_ref, dyl_ref, du_ref, dw_ref, db_ref, dd_ref):
        @pl.when(pl.program_id(0) == 0)
        def _():
            dw_ref[...] = jnp.zeros_like(dw_ref)
            db_ref[...] = jnp.zeros_like(db_ref)
            dd_ref[...] = jnp.zeros_like(dd_ref)

        ylv, dyv, wv = yl_ref[...], dy_ref[...].astype(F32), w_ref[...]
        yg, dgelu = _gelu_and_grad(ylv)
        ygb = yg.astype(BF16)
        z = jnp.dot(ygb, wv, preferred_element_type=F32) + b_ref[...]
        sg = _sigmoid(z)
        dz = dyv * yg * sg * (1.0 - sg)
        dzb = dz.astype(BF16)
        dyg = dyv * sg + lax.dot_general(dzb, wv, (((1,), (1,)), ((), ())), preferred_element_type=F32)
        dyl = dyg * dgelu
        dyl_ref[...] = dyl.astype(BF16)
        du_ref[...] = dyl * d_ref[...]
        dw_ref[...] += lax.dot_general(ygb, dzb, (((0,), (0,)), ((), ())), preferred_element_type=F32)
        db_ref[...] += jnp.sum(dz, axis=0, keepdims=True)
        dd_ref[...] += jnp.sum(dyl * u_ref[...], axis=0, keepdims=True)

    blk = pl.BlockSpec((tr, SSM_WIDTH), lambda i: (i, 0))
    vec = pl.BlockSpec((1, SSM_WIDTH), lambda i: (0, 0))
    wsp = pl.BlockSpec((SSM_WIDTH, SSM_WIDTH), lambda i: (0, 0))
    return pl.pallas_call(
        body, name=name, grid=(t // tr,), in_specs=[blk, blk, vec, wsp, vec, blk],
        out_specs=[blk, blk, wsp, vec, vec],
        out_shape=[jax.ShapeDtypeStruct((t, SSM_WIDTH), BF16), jax.ShapeDtypeStruct((t, SSM_WIDTH), F32),
                   jax.ShapeDtypeStruct((SSM_WIDTH, SSM_WIDTH), F32), jax.ShapeDtypeStruct((1, SSM_WIDTH), F32),
                   jax.ShapeDtypeStruct((1, SSM_WIDTH), F32)],
        compiler_params=_params(("arbitrary",)))(yl, u, dvec, wg, bg, dyb)


def _mesh_pos():
    return lax.axis_index("x"), lax.axis_index("y"), lax.axis_index("c")


def _device_index():
    x, y, c = _mesh_pos()
    return 4 * x + 2 * y + c


def _gather_comm(arrays):
    na = len(arrays)

    def own_copy(ins, outs, sems, ai):
        return pltpu.make_async_copy(ins[ai], outs[ai].at[_device_index()], sems[2].at[ai])

    def ctx(ins, outs, sems):
        send_sems, recv_sems = sems[:2]
        x, y, c = _mesh_pos()
        chips = [(1 - x, y), (x, 1 - y), (1 - x, 1 - y)]

        def copy(ai, kk, block, to, own=False):
            slot = outs[ai].at[4 * block[0] + 2 * block[1] + block[2]]
            return pltpu.make_async_remote_copy(
                src_ref=ins[ai] if own else slot, dst_ref=slot, send_sem=send_sems.at[ai, kk],
                recv_sem=recv_sems.at[ai, kk], device_id=to, device_id_type=MESH)

        return (x, y, c), (x, y, 1 - c), chips, c, copy

    def start(ins, outs, sems):
        me, sibling, chips, c, copy = ctx(ins, outs, sems)
        for ai in range(na):
            copy(ai, 0, me, sibling, own=True).start()
            for j, chip in enumerate(chips):
                copy(ai, 1 + j, me, (*chip, c), own=True).start()
        for ai in range(na):
            own_copy(ins, outs, sems, ai).start()

    def mid(ins, outs, sems):
        me, sibling, chips, c, copy = ctx(ins, outs, sems)
        for ai in range(na):
            for j, chip in enumerate(chips):
                copy(ai, 1 + j, (*chip, c), me).wait_recv()
                copy(ai, 4 + j, (*chip, c), sibling).start()

    def end(ins, outs, sems):
        me, sibling, chips, c, copy = ctx(ins, outs, sems)
        for ai in range(na):
            copy(ai, 0, sibling, me).wait_recv()
            copy(ai, 0, me, sibling, own=True).wait_send()
            for j, chip in enumerate(chips):
                copy(ai, 4 + j, (*chip, 1 - c), me).wait_recv()
                copy(ai, 1 + j, me, (*chip, c), own=True).wait_send()
                copy(ai, 4 + j, (*chip, c), sibling).wait_send()
            own_copy(ins, outs, sems, ai).wait()

    return Comm(arrays, [jax.ShapeDtypeStruct((N_DEV,) + a.shape, a.dtype) for a in arrays],
                [pltpu.SemaphoreType.DMA((na, 7)), pltpu.SemaphoreType.DMA((na, 7)), pltpu.SemaphoreType.DMA((na,))],
                start, end, mid)


def _sequencer_gather(arrays, name, collective_id):
    comm = _gather_comm(arrays)
    na = len(arrays)

    def body(*refs):
        ins, outs, sems = refs[:na], refs[na:2 * na], refs[2 * na:]
        x, y, c = _mesh_pos()
        peers = [(x, y, 1 - c), (1 - x, y, c), (x, 1 - y, c), (1 - x, 1 - y, c)]
        barrier = pltpu.get_barrier_semaphore()
        for peer in peers:
            pl.semaphore_signal(barrier, inc=1, device_id=peer, device_id_type=MESH)
        pl.semaphore_wait(barrier, len(peers))
        comm.start(ins, outs, sems)
        comm.mid(ins, outs, sems)
        comm.end(ins, outs, sems)

    return list(pl.kernel(
        body, out_type=tuple(comm.out_shapes), mesh=plsc.ScalarSubcoreMesh(axis_name="sequencer", num_cores=1),
        name=name, scratch_types=tuple(comm.sems),
        compiler_params=pltpu.CompilerParams(collective_id=collective_id))(*arrays))


def _sequencer_exchange(comm, peers_of, name, collective_id):
    na = len(comm.inputs)

    def body(*refs):
        ins, outs, sems = refs[:na], refs[na:na + len(comm.out_shapes)], refs[na + len(comm.out_shapes):]
        peers = peers_of(*_mesh_pos())
        barrier = pltpu.get_barrier_semaphore()
        for peer in peers:
            pl.semaphore_signal(barrier, inc=1, device_id=peer, device_id_type=MESH)
        pl.semaphore_wait(barrier, len(peers))
        comm.start(ins, outs, sems)
        comm.end(ins, outs, sems)

    return list(pl.kernel(
        body, out_type=tuple(comm.out_shapes), mesh=plsc.ScalarSubcoreMesh(axis_name="sequencer", num_cores=1),
        name=name, scratch_types=tuple(comm.sems),
        compiler_params=pltpu.CompilerParams(collective_id=collective_id))(*comm.inputs))


SIBLING_SWAP_ID, CHIP_EXCHANGE_ID = 9, 10


def _sequencer_swap(arrays, name):
    return _sequencer_exchange(_swap_comm(arrays), lambda x, y, c: [(x, y, 1 - c)], name, SIBLING_SWAP_ID)[0]


def _sequencer_chips(send, name):
    return _sequencer_exchange(_chips_comm(send), lambda x, y, c: [(1 - x, y, c), (x, 1 - y, c), (1 - x, 1 - y, c)],
                               name, CHIP_EXCHANGE_ID)[0]


def _swap_comm(arrays):
    na = len(arrays)
    offs = np.concatenate([[0], np.cumsum([a.shape[1] for a in arrays])]).astype(int)

    def copies(ins, outs, sems):
        x, y, c = _mesh_pos()
        return [pltpu.make_async_remote_copy(
            src_ref=ins[ai].at[2 * k + 1 - c], dst_ref=outs[0].at[k, pl.ds(int(offs[ai]), arrays[ai].shape[1])],
            send_sem=sems[0].at[ai, k], recv_sem=sems[1].at[ai, k], device_id=(x, y, 1 - c), device_id_type=MESH)
            for ai in range(na) for k in range(4)]

    def start(ins, outs, sems):
        for cp in copies(ins, outs, sems):
            cp.start()

    def end(ins, outs, sems):
        for cp in copies(ins, outs, sems):
            cp.wait()

    return Comm(arrays, [jax.ShapeDtypeStruct((4, int(offs[-1]), PACK_COLS), arrays[0].dtype)],
                [pltpu.SemaphoreType.DMA((na, 4)), pltpu.SemaphoreType.DMA((na, 4))], start, end)


def _chips_comm(send):
    def copies(ins, outs, sems):
        x, y, c = _mesh_pos()
        chips = [(1 - x, y), (x, 1 - y), (1 - x, 1 - y)]
        return [pltpu.make_async_remote_copy(
            src_ref=ins[0].at[2 * cx + cy], dst_ref=outs[0].at[j], send_sem=sems[0].at[j], recv_sem=sems[1].at[j],
            device_id=(cx, cy, c), device_id_type=MESH) for j, (cx, cy) in enumerate(chips)]

    def start(ins, outs, sems):
        for cp in copies(ins, outs, sems):
            cp.start()

    def end(ins, outs, sems):
        for cp in copies(ins, outs, sems):
            cp.wait()

    return Comm([send], [jax.ShapeDtypeStruct((3,) + send.shape[1:], send.dtype)],
                [pltpu.SemaphoreType.DMA((3,)), pltpu.SemaphoreType.DMA((3,))], start, end)


def _pair_sum(keep, recv, name, tr=464):
    nchip, rows, cols = keep.shape

    def body(g_ref, r_ref, o_ref):
        o_ref[...] = (g_ref[...].astype(F32) + r_ref[...].astype(F32)).astype(BF16)

    blk = pl.BlockSpec((1, tr, cols), lambda k, i: (k, i, 0))
    return pl.pallas_call(
        body, name=name, grid=(nchip, rows // tr), in_specs=[blk, blk], out_specs=blk,
        out_shape=jax.ShapeDtypeStruct((nchip, rows, cols), BF16),
        compiler_params=_params(("parallel", "parallel")))(keep, recv)


def _pair_sum_pieces(pieces, recv, name, tr):
    _, rows, cols = pieces.shape
    core = lax.axis_index("c").astype(jnp.int32).reshape(1)

    def body(c_ref, g_ref, r_ref, o_ref):
        del c_ref
        o_ref[...] = (g_ref[...].astype(F32) + r_ref[...].astype(F32)).astype(BF16)

    grid_spec = pltpu.PrefetchScalarGridSpec(
        num_scalar_prefetch=1, grid=(4, rows // tr),
        in_specs=[pl.BlockSpec((1, tr, cols), lambda k, i, c_ref: (2 * k + c_ref[0], i, 0)),
                  pl.BlockSpec((1, tr, cols), lambda k, i, c_ref: (k, i, 0))],
        out_specs=pl.BlockSpec((1, tr, cols), lambda k, i, c_ref: (k, i, 0)))
    return pl.pallas_call(
        body, name=name, grid_spec=grid_spec, out_shape=jax.ShapeDtypeStruct((4, rows, cols), BF16),
        compiler_params=_params(("parallel", "parallel")))(core, pieces, recv)


def _chip_sum(own, others, name, tr=464):
    _, rows, cols = own.shape
    chip = (2 * lax.axis_index("x") + lax.axis_index("y")).astype(jnp.int32).reshape(1)

    def body(chip_ref, own_ref, oth_ref, o_ref):
        del chip_ref
        acc = own_ref[0].astype(F32)
        for j in range(3):
            acc = acc + oth_ref[j].astype(F32)
        o_ref[...] = acc

    grid_spec = pltpu.PrefetchScalarGridSpec(
        num_scalar_prefetch=1, grid=(rows // tr,),
        in_specs=[pl.BlockSpec((1, tr, cols), lambda i, chip_ref: (chip_ref[0], i, 0)),
                  pl.BlockSpec((3, tr, cols), lambda i, chip_ref: (0, i, 0))],
        out_specs=pl.BlockSpec((tr, cols), lambda i, chip_ref: (i, 0)))
    return pl.pallas_call(
        body, name=name, grid_spec=grid_spec, out_shape=jax.ShapeDtypeStruct((rows, cols), F32),
        compiler_params=_params(("parallel",)))(chip, own, others)


def _sum_leading(parts, name, tr=464):
    nparts, rows, cols = parts.shape
    tr = tr if rows % tr == 0 else rows

    def body(p_ref, o_ref):
        acc = p_ref[0].astype(F32)
        for i in range(1, nparts):
            acc = acc + p_ref[i].astype(F32)
        o_ref[...] = acc

    return pl.pallas_call(
        body, name=name, grid=(rows // tr,),
        in_specs=[pl.BlockSpec((nparts, tr, cols), lambda i: (0, i, 0))],
        out_specs=pl.BlockSpec((tr, cols), lambda i: (i, 0)), out_shape=jax.ShapeDtypeStruct((rows, cols), F32),
        compiler_params=_params(("parallel",)))(parts)


def _adamw(w, g, m, v, name, comm=None):
    shape = w.shape
    cols = shape[-1]
    lead = shape[0] if len(shape) >= 3 else 1
    rows = int(np.prod(shape[:-1])) // lead if len(shape) > 1 else 1
    w2, g2, m2, v2 = (a.reshape(lead, rows, cols) for a in (w, g, m, v))
    tr = max([t for t in range(8, min(rows, 512) + 1, 8) if rows % t == 0], default=rows)
    bc1, bc2 = 1.0 - ADAM_B1 ** ADAM_STEP, 1.0 - ADAM_B2 ** ADAM_STEP

    def body(w_ref, g_ref, m_ref, v_ref, d_ref, nm_ref, nv_ref):
        gv = g_ref[...]
        nm = ADAM_B1 * m_ref[...] + (1.0 - ADAM_B1) * gv
        nv = ADAM_B2 * v_ref[...] + (1.0 - ADAM_B2) * (gv * gv)
        nm_ref[...] = nm
        nv_ref[...] = nv
        d_ref[...] = -ADAM_LR * ((nm / bc1) / (jnp.sqrt(nv / bc2) + ADAM_EPS) + ADAM_WD * w_ref[...])

    blk = pl.BlockSpec((1, tr, cols), lambda l, i: (l, i, 0))
    res = _call(body, name=name, grid=(lead, rows // tr), in_specs=[blk] * 4, out_specs=[blk] * 3,
                out_shape=[jax.ShapeDtypeStruct((lead, rows, cols), F32)] * 3, sem=("parallel", "parallel"),
                args=(w2, g2, m2, v2), comm=comm)
    outs, couts = res if comm is not None else (res, None)
    outs = tuple(o.reshape(shape) for o in outs)
    return outs if comm is None else (outs, couts)


WEIGHT_NAMES = ['norm_mix_g', 'norm_xa_g', 'norm_ffn_g', 'norm_mem_g', 'norm_final_g', 'w_in_ab', 'conv_qkv_a',
                'a_log_a', 'dt_bias_a', 'onorm_g_a', 'ssm_lambda_re', 'ssm_lambda_im', 'ssm_b_re', 'ssm_b_im',
                'ssm_c_re', 'ssm_c_im', 'ssm_d', 'ssm_log_dt', 'w_glu_b', 'b_glu_b', 'w_out_ab', 'pool_w',
                'pool_scale', 'xa_wq', 'xa_wkv', 'xa_wo', 'ffn_w_up', 'ffn_conv', 'ffn_w_down']
BIG_SHARDED = {'w_in_ab': ((1, 1024, 2568), 2), 'w_glu_b': ((1, 512, 512), 1), 'w_out_ab': ((1, 1024, 1024), 1),
               'pool_w': ((1, 4, 256, 256), 2), 'xa_wq': ((2, 1024, 1024), 1), 'xa_wkv': ((2, 1024, 2048), 2),
               'xa_wo': ((2, 1024, 1024), 1), 'ffn_w_up': ((2, 1024, 5632), 2), 'ffn_w_down': ((2, 2816, 1024), 1)}
SMALL_SHARDED = {'conv_qkv_a': ((1, 4, 1536), 2), 'pool_scale': ((1, 1024), 1), 'ffn_conv': ((2, 3, 5632), 2)}
REPLICATED = {'norm_mix_g': (2, 1024), 'norm_xa_g': (2, 1024), 'norm_ffn_g': (2, 1024), 'norm_mem_g': (1024,),
              'norm_final_g': (1024,), 'a_log_a': (1, 4), 'dt_bias_a': (1, 4), 'onorm_g_a': (1, 128),
              'ssm_lambda_re': (1, 32, 64), 'ssm_lambda_im': (1, 32, 64), 'ssm_b_re': (1, 32, 64, 16),
              'ssm_b_im': (1, 32, 64, 16), 'ssm_c_re': (1, 32, 16, 64), 'ssm_c_im': (1, 32, 16, 64),
              'ssm_d': (1, 32, 16), 'ssm_log_dt': (1, 32), 'b_glu_b': (1, 512)}
PACK_ROW_ALIGN = 8


def _shard_shape(shape, axis):
    return tuple(s // N_DEV if i == axis else s for i, s in enumerate(shape))


def _round_up(n, m):
    return (n + m - 1) // m * m


def _pack(arrays):
    total = sum(int(np.prod(a.shape)) for a in arrays)
    padded = _round_up(total, PACK_COLS * PACK_ROW_ALIGN)
    parts = [a.astype(F32).reshape(-1) for a in arrays]
    if padded != total:
        parts.append(jnp.zeros((padded - total,), F32))
    return jnp.concatenate(parts).reshape(padded // PACK_COLS, PACK_COLS)


def _unpack(packed, shapes):
    flat, out, off = packed.reshape(-1), [], 0
    for shape in shapes:
        size = int(np.prod(shape))
        out.append(flat[off:off + size].reshape(shape))
        off += size
    return out


def _split_shards(full, axis):
    shape = full.shape
    s = shape[axis] // N_DEV
    a = full.reshape(shape[:axis] + (N_DEV, s) + shape[axis + 1:])
    return jnp.moveaxis(a, axis, 0).reshape(N_DEV, -1)


def _merge_shards(pieces, shape, axis):
    sh = _shard_shape(shape, axis)
    a = pieces.reshape((N_DEV,) + sh)
    a = jnp.moveaxis(a, 0, axis)
    return a.reshape(shape)


_SCAN_NB = SSM_CH // SCAN_CB


def _to_scan_layout(m, axis):
    shape = m.shape
    m = m.reshape(shape[:axis] + (2, _SCAN_NB, SCAN_CB) + shape[axis + 1:])
    return jnp.swapaxes(m, axis, axis + 1).reshape(shape)


def _from_scan_layout(m, axis):
    shape = m.shape
    m = m.reshape(shape[:axis] + (_SCAN_NB, 2, SCAN_CB) + shape[axis + 1:])
    return jnp.swapaxes(m, axis, axis + 1).reshape(shape)


def _s5_discretise(lam_re, lam_im, b_re, b_im, log_dt):
    dt = jnp.exp(log_dt)[:, None]
    mag = jnp.exp(lam_re * dt)
    ang = lam_im * dt
    lb_re, lb_im = mag * jnp.cos(ang), mag * jnp.sin(ang)
    den = lam_re * lam_re + lam_im * lam_im
    nr, ni = lb_re - 1.0, lb_im
    coef_re = (nr * lam_re + ni * lam_im) / den
    coef_im = (ni * lam_re - nr * lam_im) / den
    bb_re = coef_re[..., None] * b_re - coef_im[..., None] * b_im
    bb_im = coef_re[..., None] * b_im + coef_im[..., None] * b_re
    return lb_re, lb_im, bb_re, bb_im


_GROUPS_PER_BLOCK = N_GROUPS // _SCAN_NB
_U_BLOCK = _GROUPS_PER_BLOCK * SSM_GROUP


def _s5_matrices(lb_re, lb_im, bb_re, bb_im, c_re, c_im):
    eye = jnp.eye(_GROUPS_PER_BLOCK, dtype=F32)
    blocked = lambda m: m.reshape((_SCAN_NB, _GROUPS_PER_BLOCK) + m.shape[1:])
    bmat = lambda bb: jnp.einsum('jgph,gk->jghkp', blocked(bb), eye).reshape(_SCAN_NB, _U_BLOCK, SCAN_CB)
    cmat = lambda cc: jnp.einsum('jghp,gk->jkpgh', blocked(cc), eye).reshape(_SCAN_NB, SCAN_CB, _U_BLOCK)
    b_in = jnp.concatenate([bmat(bb_re), bmat(bb_im)], axis=2)
    c_out = jnp.concatenate([cmat(c_re), -cmat(c_im)], axis=1)
    a_row = _to_scan_layout(jnp.concatenate([lb_re.reshape(1, SSM_CH), lb_im.reshape(1, SSM_CH)], axis=1), 1)
    return b_in, c_out, a_row


def _s5_matrix_grads(db_in, dc_out, da_row):
    da_nat = _from_scan_layout(da_row, 1)
    eye = jnp.eye(_GROUPS_PER_BLOCK, dtype=F32)
    nb, gb = _SCAN_NB, _GROUPS_PER_BLOCK
    bgrad = lambda m: jnp.einsum('jghkp,gk->jgph', m.reshape(nb, gb, SSM_GROUP, gb, SSM_STATE), eye
                                 ).reshape(N_GROUPS, SSM_STATE, SSM_GROUP)
    cgrad = lambda m: jnp.einsum('jkpgh,gk->jghp', m.reshape(nb, gb, SSM_STATE, gb, SSM_GROUP), eye
                                 ).reshape(N_GROUPS, SSM_GROUP, SSM_STATE)
    dbb_re, dbb_im = bgrad(db_in[:, :, :SCAN_CB]), bgrad(db_in[:, :, SCAN_CB:])
    dc_re, dc_im = cgrad(dc_out[:, :SCAN_CB]), -cgrad(dc_out[:, SCAN_CB:])
    dlb_re = da_nat[0, :SSM_CH].reshape(N_GROUPS, SSM_STATE)
    dlb_im = da_nat[0, SSM_CH:].reshape(N_GROUPS, SSM_STATE)
    return dlb_re, dlb_im, dbb_re, dbb_im, dc_re, dc_im


def _as_pieces(a):
    return a.reshape(N_DEV, a.shape[0] // N_DEV, a.shape[1])


def _hybrid_fwd(xn, x, wts, p, weights, riders):
    sv = {}
    hq = _mm(xn, wts['w_qkv_t'], "nt", "l0_in_qkv")
    gate = _mm(xn, wts['w_gate_t'], "nt", "l0_in_gate")
    ba = _mm(xn, wts['w_ba_t'], "nt", "l0_in_ba")
    u = _mm(xn, wts['w_u_t'], "nt", "l0_in_u")
    conv = p['conv_qkv']
    qkv = _qkv_pre_fwd(hq, conv, "l0_qkv_pre")
    gates = _gates_fwd(ba, p['arow'], p['brow'], "l0_gates")
    o, tm_all, s_all = riders.run("l0_gdr_fwd", _gdr_fwd, qkv, gates)
    wts['w_glu'], wts['w_out'] = weights.full['w_glu'], weights.full['w_out']
    y_a = _onorm_fwd(o, gate, p['onorm_g'], "l0_onorm")
    bu = riders.run("l0_s5_bu", _mm_bd, u, p['b_in'], "nn")
    xs = riders.run("l0_s5_scan", _s5_scan_fwd, bu, p['a_row'])
    weights.gather_by_sequencer(GATHER_LAYER1, xs, "gather_layer1", GATHER_LAYER1_ID)
    yc = riders.run("l0_s5_cx", _mm_bd, xs, p['c_out'], "nn")
    yl, y_b = _glu_fwd(yc, u, p['d_row'], wts['w_glu'], p['b_glu'], "l0_glu")
    mixed = jnp.concatenate([y_a, y_b], axis=1)
    x1 = _mm(mixed, wts['w_out'], "nn", "l0_out", res=x)
    sv.update(hq=hq, gate=gate, ba=ba, u=u, qkv=qkv, gb=gates, o=o, tm=tm_all, s=s_all, xs=xs, yl=yl, mixed=mixed)
    return x1, sv


def _hybrid_bwd(dx1, xn, wts, p, sv, riders):
    gr = {}
    dmixed = _mm(dx1, wts['w_out'], "nt", "l0_out_dx", out_dtype=BF16)
    riders.grad('w_out', _as_pieces(_mm(sv['mixed'], dx1, "tn", "l0_out_dw", out_dtype=BF16)))
    dya, dyb = dmixed[:, :WIDTH_A], dmixed[:, WIDTH_A:]
    dyl, du_direct, dw_glu, gr['b_glu_b'], dd = _glu_bwd(
        sv['yl'], sv['u'], p['d_row'], wts['w_glu'], p['b_glu'], dyb, "l0_glu_bwd")
    riders.grad('w_glu', dw_glu.astype(BF16).reshape(N_DEV, -1, PACK_COLS))
    dxs = riders.run("l0_s5_cx_dx", _mm_bd, dyl, p['c_out'], "nt")
    dc_out = _mm_bd(sv['xs'], dyl, "tn", "l0_s5_cx_dw")
    lam, da_row = riders.run("l0_s5_scan_bwd", _s5_scan_bwd, dxs, sv['xs'], p['a_row'])
    du = _mm_bd(lam, p['b_in'], "nt", "l0_s5_bu_dx", res=du_direct, out_dtype=BF16)
    db_in = _mm_bd(sv['u'], lam, "tn", "l0_s5_bu_dw")
    gr['s5'] = (db_in, dc_out, da_row, dd)
    do, dgate, gr['onorm_g_a'] = _onorm_bwd(sv['o'], sv['gate'], p['onorm_g'], dya, "l0_onorm_bwd")
    dqkv, dgb = riders.run("l0_gdr_bwd", _gdr_bwd, sv['qkv'], sv['gb'], sv['tm'], sv['s'], do)
    dhq, gr['conv_qkv_a'] = _qkv_pre_bwd(sv['hq'], p['conv_qkv'], dqkv, "l0_qkv_pre_bwd")
    dba, da_log, ddt_bias = _gates_bwd(sv['ba'], p['arow'], p['brow'], dgb, "l0_gates_bwd")
    gr['a_log_a'], gr['dt_bias_a'] = da_log[:, 4:8], ddt_bias[:, 4:8]
    dw_qkv_t = _mm(dhq, xn, "tn", "l0_in_qkv_dw", out_dtype=BF16)
    dw_gate_t = _mm(dgate, xn, "tn", "l0_in_gate_dw", out_dtype=BF16)
    dw_ba_t = _mm(dba, xn, "tn", "l0_in_ba_dw", out_dtype=BF16)
    dw_u_t = _mm(du, xn, "tn", "l0_in_u_dw", out_dtype=BF16)
    dw_in_t = _as_pieces(jnp.concatenate([dw_qkv_t, dw_gate_t, dw_ba_t[:8], dw_u_t], axis=0))
    riders.grad('w_in_t', jnp.concatenate(
        [dw_in_t, jnp.zeros((N_DEV, dict(PIECES)['w_in_t'] - W_IN_PIECE, D_MODEL), BF16)], axis=1))
    dxn = riders.run("l0_in_qkv_dx", _mm, dhq, wts['w_qkv_t'], "nn")
    dxn = _mm(dgate, wts['w_gate_t'], "nn", "l0_in_gate_dx", res=dxn)
    dxn = _mm(dba, wts['w_ba_t'], "nn", "l0_in_ba_dx", res=dxn)
    dxn = riders.run("l0_in_u_dx", _mm, du, wts['w_u_t'], "nn", res=dxn)
    return dxn, gr


def _xa_fwd(x1, g, mem_n, wq, wkv_t, wo, tag, riders):
    xq = _rms_fwd(x1, g, BF16, tag + "_norm")
    q = _mm(xq, wq, "nn", tag + "_q", out_dtype=BF16)
    kv = _mm(mem_n, wkv_t, "nt", tag + "_kv", out_dtype=BF16)
    o = riders.run(tag + "_attn", _attn_fwd, q, kv)
    x2 = _mm(o, wo, "nn", tag + "_o", res=x1)
    return x2, dict(xq=xq, q=q, kv=kv, o=o)


def _xa_bwd(dx2, x1, g, mem_n, wq, wkv_t, wo, sv, tag, layer, riders):
    do = _mm(dx2, wo, "nt", tag + "_o_dx", out_dtype=BF16)
    riders.grad('wo%d' % layer, _as_pieces(_mm(sv['o'], dx2, "tn", tag + "_o_dw", out_dtype=BF16)))
    dq, dk, dv = _attn_bwd(sv['q'], sv['kv'], do, tag + "_attn_bwd")
    dkv = jnp.concatenate([dk, dv], axis=1).astype(BF16)
    dxq = _mm(dq, wq, "nt", tag + "_q_dx")
    riders.grad('wq%d' % layer, _as_pieces(_mm(sv['xq'], dq, "tn", tag + "_q_dw", out_dtype=BF16)))
    dmem_n = _mm(dkv, wkv_t, "nn", tag + "_kv_dx")
    riders.grad('wkv_t%d' % layer, _as_pieces(_mm(dkv, mem_n, "tn", tag + "_kv_dw", out_dtype=BF16)))
    dx1, dg = riders.run(tag + "_norm_bwd", _rms_bwd, x1, g, dxq, dx2)
    return dx1, dmem_n, dg


def _ffn_fwd(x2, g, w_up_t, conv, w_down, tag, riders):
    xf = _rms_fwd(x2, g, BF16, tag + "_norm")
    h = riders.run(tag + "_up", _mm, xf, w_up_t, "nt")
    a = riders.run(tag + "_act", _ffn_act_fwd, h, conv)
    x3 = _mm(a, w_down, "nn", tag + "_down", res=x2)
    return x3, dict(xf=xf, h=h, a=a)


def _ffn_bwd(dx3, x2, g, w_up_t, conv, w_down, sv, tag, layer, riders):
    da = _mm(dx3, w_down, "nt", tag + "_down_dx")
    riders.grad('down%d' % layer, _as_pieces(_mm(sv['a'], dx3, "tn", tag + "_down_dw", out_dtype=BF16)))
    dh, dconv = riders.run(tag + "_act_bwd", _ffn_act_bwd, sv['h'], conv, da)
    dxf = riders.run(tag + "_up_dx", _mm_parts, dh, w_up_t, "nn")
    dw_up_t = riders.run(tag + "_up_dw", _mm_parts, dh, sv['xf'], "tn", out_dtype=BF16)
    riders.grad('up_t%d' % layer, _as_pieces(dw_up_t))
    dx2, dg = riders.run(tag + "_norm_bwd", _rms_bwd, x2, g, dxf, dx3)
    return dx2, dconv, dg


BIG_NAMES, SMALL_NAMES, REP_NAMES = list(BIG_SHARDED), list(SMALL_SHARDED), list(REPLICATED)
SMALL_SIZES = [int(np.prod(_shard_shape(*SMALL_SHARDED[n]))) for n in SMALL_NAMES]


PIECES = [('w_in_t', 384), ('w_glu', 32), ('w_out', 128), ('pool_w', 32), ('wq0', 128), ('wq1', 128),
          ('wkv_t0', 256), ('wkv_t1', 256), ('wo0', 128), ('wo1', 128), ('up_t0', 704), ('up_t1', 704),
          ('down0', 352), ('down1', 352)]
W_IN_ROWS = 4 * WIDTH_A + 2 * N_HEADS_A + SSM_WIDTH
W_IN_PIECE = W_IN_ROWS // N_DEV


def _row_tile(rows):
    return max(t for t in range(16, min(rows, 512) + 1, 16) if rows % t == 0)


class _Riders:
    def __init__(self):
        self.waiting = {}
        self.deferred = {}
        self.grads = {}
        self.groups = []
        self.reduced = {}

    def add(self, host, comm, then):
        self.waiting.setdefault(host, []).append((comm, then))

    def after(self, marker, then):
        self.deferred.setdefault(marker, []).append(then)

    def mark(self, name, out=None):
        for cont in self.deferred.pop(name, []):
            step = cont()
            if step is not None:
                values, then = step
                out, values = lax.optimization_barrier((out, values))
                then(values)
        return out

    def run(self, name, fn, *args, **kw):
        riders = self.waiting.pop(name, [])
        if not riders:
            out = fn(*args, name=name, **kw)
        else:
            out, couts = fn(*args, name=name, comm=[c for c, _ in riders], **kw)
            for (_, then), got in zip(riders, couts):
                then(got)
        return self.mark(name, out)

    def grad(self, key, pieces):
        self.grads[key] = pieces
        for group in [g for g in self.groups if all(k in self.grads for k in g[1])]:
            self.groups.remove(group)
            self._reduce(*group)

    def _reduce(self, name, keys, pair_marker, sum_marker):
        arrays = [self.grads[k] for k in keys]
        rows = sum(a.shape[1] for a in arrays)
        tile = _row_tile(rows)
        from_sibling = _sequencer_swap(arrays, name + "_to_sibling")

        def after_swap():
            if len(arrays) == 1:
                chip_sums = _pair_sum_pieces(arrays[0], from_sibling, name + "_pair_sum", tr=tile)
            else:
                core = lax.axis_index("c")
                keep = jnp.concatenate(
                    [lax.dynamic_index_in_dim(a.reshape(4, 2, a.shape[1], PACK_COLS), core, 1, keepdims=False)
                     for a in arrays], axis=1)
                chip_sums = _pair_sum(keep, from_sibling, name + "_pair_sum", tr=tile)

            def exchange_among_chips(chip_sums):
                from_chips = _sequencer_chips(chip_sums, name + "_to_chips")

                def store(total):
                    off = 0
                    for k, a in zip(keys, arrays):
                        self.reduced[k] = total[off:off + a.shape[1]]
                        off += a.shape[1]

                self.after(sum_marker, lambda: (
                    _chip_sum(chip_sums, from_chips, name + "_chip_sum", tr=tile), store))

            return chip_sums, exchange_among_chips

        self.after(pair_marker, after_swap)


class _Weights:
    def __init__(self, inp):
        bf = lambda a: a.astype(BF16)
        local = {'w_in_t': bf(inp['w_in_ab'][0]).T, 'w_glu': bf(inp['w_glu_b'][0]), 'w_out': bf(inp['w_out_ab'][0]),
                 'pool_w': bf(inp['pool_w'][0]),
                 'small': _pack([inp[n] for n in SMALL_NAMES])}
        for l in range(2):
            local['wq%d' % l] = bf(inp['xa_wq'][l])
            local['wkv_t%d' % l] = bf(inp['xa_wkv'][l]).T
            local['wo%d' % l] = bf(inp['xa_wo'][l])
            local['up_t%d' % l] = bf(inp['ffn_w_up'][l]).T
            local['down%d' % l] = bf(inp['ffn_w_down'][l])
        self.local, self.full = local, {}

    def plan(self, keys):
        return _gather_comm([self.local[k] for k in keys])

    def gather_by_sequencer(self, keys, after, name, collective_id):
        arrays = [self.local[k] for k in keys]
        tie = (after.reshape(-1)[0] * 0.0).astype(arrays[0].dtype)
        arrays[0] = arrays[0] + tie
        self.land(keys, _sequencer_gather(arrays, name, collective_id))

    def land(self, keys, gathered):
        for k, g in zip(keys, gathered):
            if k == 'small':
                off = 0
                for n, size in zip(SMALL_NAMES, SMALL_SIZES):
                    self.full[n] = _merge_shards(g.reshape(N_DEV, -1)[:, off:off + size], *SMALL_SHARDED[n])
                    off += size
            elif k == 'pool_w':
                self.full[k] = jnp.swapaxes(g, 0, 1).reshape(len(POOL_WINDOWS), POOL_GROUP, POOL_GROUP)
            else:
                self.full[k] = g.reshape(N_DEV * g.shape[1], g.shape[2])


GATHER_FIRST = ['w_in_t', 'small']
GATHER_LAYER0 = ['w_glu', 'w_out', 'wq0', 'wkv_t0', 'wo0', 'down0', 'up_t0']
GATHER_LAYER1 = ['pool_w', 'wq1', 'wkv_t1', 'wo1', 'up_t1', 'down1']
GATHER_LAYER0_ID, GATHER_LAYER1_ID = 7, 8
GRAD_RIDES = [('g_down1', ['down1'], 'l1_ffn_up_dx', 'l1_xa_norm_bwd'),
              ('g_up1', ['up_t1'], 'l1_xa_norm_bwd', 'l0_ffn_up_dx'),
              ('g_xa1', ['wq1', 'wkv_t1', 'wo1', 'pool_w'], 'l0_ffn_up_dx', 'l0_xa_norm_bwd'),
              ('g_down0', ['down0'], 'l0_ffn_up_dx', 'l0_s5_scan_bwd'),
              ('g_l0', ['up_t0', 'wq0', 'wkv_t0', 'wo0'], 'l0_s5_cx_dx', 'l0_in_u_dx'),
              ('g_out', ['w_out', 'w_glu'], 'l0_s5_scan_bwd', 'l0_in_u_dx'),
              ('g_in', ['w_in_t'], 'l0_in_u_dx', 'adamw_pool_w')]


def _local_step(inp):
    f32_of = lambda n: inp[n].astype(F32)
    weights = _Weights(inp)
    riders = _Riders()
    riders.groups = list(GRAD_RIDES)
    full = weights.full
    weights.land(GATHER_FIRST, _comm_only(weights.plan(GATHER_FIRST), "gather_first"))
    weights.gather_by_sequencer(GATHER_LAYER0, full['w_in_t'], "gather_layer0", GATHER_LAYER0_ID)
    w_in_t = full['w_in_t']
    wts0 = dict(w_qkv_t=w_in_t[:3 * WIDTH_A], w_gate_t=w_in_t[3 * WIDTH_A:4 * WIDTH_A],
                w_ba_t=jnp.concatenate([w_in_t[4 * WIDTH_A:4 * WIDTH_A + 8], jnp.zeros((LANE - 8, D_MODEL), BF16)], 0),
                w_u_t=w_in_t[4 * WIDTH_A + 8:])
    lb_disc, disc_vjp = jax.vjp(_s5_discretise, f32_of('ssm_lambda_re')[0], f32_of('ssm_lambda_im')[0],
                                f32_of('ssm_b_re')[0], f32_of('ssm_b_im')[0], f32_of('ssm_log_dt')[0])
    b_in, c_out, a_row = _s5_matrices(*lb_disc, f32_of('ssm_c_re')[0], f32_of('ssm_c_im')[0])
    zeros4 = jnp.zeros((1, 4), F32)
    p0 = dict(conv_qkv=full['conv_qkv_a'][0], onorm_g=f32_of('onorm_g_a'),
              arow=jnp.concatenate([zeros4, f32_of('a_log_a'), jnp.zeros((1, LANE - 8), F32)], 1),
              brow=jnp.concatenate([zeros4, f32_of('dt_bias_a'), jnp.zeros((1, LANE - 8), F32)], 1),
              b_in=b_in.astype(BF16), c_out=c_out.astype(BF16), a_row=a_row,
              d_row=f32_of('ssm_d').reshape(1, SSM_WIDTH), b_glu=f32_of('b_glu_b'))

    x0 = inp['x'][0]
    mem_n = _rms_fwd(inp['mem'][0], inp['norm_mem_g'], BF16, "mem_norm")
    xn0 = _rms_fwd(x0, inp['norm_mix_g'][0], BF16, "l0_mix_norm")
    x1, sv_mix0 = _hybrid_fwd(xn0, x0, wts0, p0, weights, riders)
    x2, sv_xa0 = _xa_fwd(x1, inp['norm_xa_g'][0], mem_n, full['wq0'], full['wkv_t0'], full['wo0'], "l0_xa", riders)
    x3, sv_ffn0 = _ffn_fwd(x2, inp['norm_ffn_g'][0], full['up_t0'], full['ffn_conv'][0], full['down0'], "l0_ffn", riders)
    xn1 = _rms_fwd(x3, inp['norm_mix_g'][1], F32, "l1_mix_norm")
    x4 = _pool_fwd(xn1, full['pool_w'], full['pool_scale'], x3, "l1_pool")
    x5, sv_xa1 = _xa_fwd(x4, inp['norm_xa_g'][1], mem_n, full['wq1'], full['wkv_t1'], full['wo1'], "l1_xa", riders)
    x6, sv_ffn1 = _ffn_fwd(x5, inp['norm_ffn_g'][1], full['up_t1'], full['ffn_conv'][1], full['down1'], "l1_ffn", riders)
    loss_part, dx6, dg_final = _loss_head(x6, inp['norm_final_g'], inp['loss_target'][0], "loss_head")

    dx5, dconv1, dg_ffn1 = _ffn_bwd(dx6, x5, inp['norm_ffn_g'][1], full['up_t1'], full['ffn_conv'][1], full['down1'],
                                    sv_ffn1, "l1_ffn", 1, riders)
    dx4, dmem1, dg_xa1 = _xa_bwd(dx5, x4, inp['norm_xa_g'][1], mem_n, full['wq1'], full['wkv_t1'], full['wo1'],
                                 sv_xa1, "l1_xa", 1, riders)
    dxn1, dpool_w, dpool_scale = _pool_bwd(xn1, full['pool_w'], full['pool_scale'], dx4, "l1_pool_bwd")
    pool_pieces = jnp.swapaxes(dpool_w.astype(BF16).reshape(len(POOL_WINDOWS), N_DEV, -1, POOL_GROUP), 0, 1)
    riders.grad('pool_w', pool_pieces.reshape(N_DEV, -1, PACK_COLS))
    dx3, dg_mix1 = riders.run("l1_mix_norm_bwd", _rms_bwd, x3, inp['norm_mix_g'][1], dxn1, dx4)
    dx2, dconv0, dg_ffn0 = _ffn_bwd(dx3, x2, inp['norm_ffn_g'][0], full['up_t0'], full['ffn_conv'][0], full['down0'],
                                    sv_ffn0, "l0_ffn", 0, riders)
    dx1, dmem0, dg_xa0 = _xa_bwd(dx2, x1, inp['norm_xa_g'][0], mem_n, full['wq0'], full['wkv_t0'], full['wo0'],
                                 sv_xa0, "l0_xa", 0, riders)
    dxn0, g_mix0 = _hybrid_bwd(dx1, xn0, wts0, p0, sv_mix0, riders)
    grad_x, dg_mix0 = _rms_bwd(x0, inp['norm_mix_g'][0], dxn0, dx1, "l0_mix_norm_bwd")
    _, dg_mem = _rms_bwd(inp['mem'][0], inp['norm_mem_g'], dmem0 + dmem1, None, "mem_norm_bwd")
    assert not riders.groups and not riders.waiting and all(k.startswith("adamw_") for k in riders.deferred), (
        riders.groups, list(riders.waiting), list(riders.deferred))

    db_in, dc_out, da_row, dd = g_mix0['s5']
    dlb_re, dlb_im, dbb_re, dbb_im, dc_re, dc_im = _s5_matrix_grads(db_in, dc_out, da_row)
    dlam_re, dlam_im, dbr, dbi, dlog_dt = disc_vjp((dlb_re, dlb_im, dbb_re, dbb_im))

    rep_grads = {
        'norm_mix_g': jnp.concatenate([dg_mix0, dg_mix1], 0), 'norm_xa_g': jnp.concatenate([dg_xa0, dg_xa1], 0),
        'norm_ffn_g': jnp.concatenate([dg_ffn0, dg_ffn1], 0), 'norm_mem_g': dg_mem.reshape(-1),
        'norm_final_g': dg_final.reshape(-1), 'a_log_a': g_mix0['a_log_a'], 'dt_bias_a': g_mix0['dt_bias_a'],
        'onorm_g_a': g_mix0['onorm_g_a'], 'ssm_lambda_re': dlam_re[None], 'ssm_lambda_im': dlam_im[None],
        'ssm_b_re': dbr[None], 'ssm_b_im': dbi[None], 'ssm_c_re': dc_re[None], 'ssm_c_im': dc_im[None],
        'ssm_d': dd.reshape(1, N_GROUPS, SSM_GROUP), 'ssm_log_dt': dlog_dt[None], 'b_glu_b': g_mix0['b_glu_b']}
    small_grads = {'conv_qkv_a': g_mix0['conv_qkv_a'][None], 'pool_scale': dpool_scale,
                   'ffn_conv': jnp.stack([dconv0, dconv1])}
    return loss_part, grad_x, riders, rep_grads, small_grads


ADAMW_ORDER = ['ffn_w_up', 'ffn_w_down', 'xa_wkv', 'xa_wq', 'xa_wo', 'w_out_ab', 'w_glu_b', 'pool_w', 'w_in_ab']


def _update(inp, loss_part, grad_x, riders, rep_grads, small_grads):
    dev = _device_index()
    misc_local = _pack([rep_grads[n] for n in REP_NAMES] + [small_grads[n] for n in SMALL_NAMES] + [loss_part])
    (misc_all,) = _sequencer_gather([misc_local], "gather_small_grads", GATHER_LAYER0_ID)
    piece = lambda key: riders.reduced[key]
    both = lambda name: jnp.stack([piece(name + '0'), piece(name + '1')])
    swap = lambda a: jnp.swapaxes(a, -1, -2)
    reduced = {'w_in_ab': lambda: piece('w_in_t')[:W_IN_PIECE][None],
               'w_glu_b': lambda: piece('w_glu').reshape(inp['w_glu_b'].shape),
               'w_out_ab': lambda: piece('w_out')[None], 'pool_w': lambda: piece('pool_w').reshape(inp['pool_w'].shape),
               'xa_wq': lambda: both('wq'), 'xa_wkv': lambda: both('wkv_t'), 'xa_wo': lambda: both('wo'),
               'ffn_w_up': lambda: both('up_t'), 'ffn_w_down': lambda: both('down')}
    transposed = ('w_in_ab', 'xa_wkv', 'ffn_w_up')
    grads, upd = {}, {}
    assert sorted(ADAMW_ORDER) == sorted(BIG_NAMES)
    for n in ADAMW_ORDER:
        fix = swap if n in transposed else (lambda a: a)
        g = reduced[n]()
        out = riders.run("adamw_" + n, _adamw, fix(inp[n]), g, fix(inp['m_' + n]), fix(inp['v_' + n]))
        upd[n], grads[n] = tuple(fix(o) for o in out), fix(g)
    assert not riders.waiting and not riders.deferred, (list(riders.waiting), list(riders.deferred))
    misc_sum = _sum_leading(misc_all, "small_grads_sum")
    misc = _unpack(misc_sum, [inp[n].shape for n in REP_NAMES] + [SMALL_SHARDED[n][0] for n in SMALL_NAMES] + [()])
    loss = misc.pop()
    for n, g in zip(REP_NAMES, misc):
        grads[n] = g
    for n, g in zip(SMALL_NAMES, misc[len(REP_NAMES):]):
        grads[n] = lax.dynamic_index_in_dim(_split_shards(g, SMALL_SHARDED[n][1]), dev, 0, keepdims=False
                                            ).reshape(inp[n].shape)
    tiny_names = REP_NAMES + SMALL_NAMES
    rep_total = sum(int(np.prod(inp[n].shape)) for n in REP_NAMES)
    packs = [_pack([inp[prefix + n] for n in tiny_names]) for prefix in ('', 'm_', 'v_')]
    g_pack = _pack([misc_sum.reshape(-1)[:rep_total]] + [grads[n] for n in SMALL_NAMES])
    tiny_out = [_unpack(o, [inp[n].shape for n in tiny_names])
                for o in _adamw(packs[0], g_pack, packs[1], packs[2], "adamw_small")]
    for i, n in enumerate(tiny_names):
        upd[n] = tuple(o[i] for o in tiny_out)

    outs = [loss, grad_x[None]]
    outs += [grads[n] for n in WEIGHT_NAMES]
    for i in range(3):
        outs += [upd[n][i] for n in WEIGHT_NAMES]
    return tuple(outs)


def _step(inp):
    loss_part, grad_x, riders, rep_grads, small_grads = _local_step(inp)
    return _update(inp, loss_part, grad_x, riders, rep_grads, small_grads)


INPUT_NAMES = (['x', 'mem'] + WEIGHT_NAMES + ['loss_target'] + ['m_' + n for n in WEIGHT_NAMES]
               + ['v_' + n for n in WEIGHT_NAMES])


def kernel(x, mem, norm_mix_g, norm_xa_g, norm_ffn_g, norm_mem_g, norm_final_g, w_in_ab, conv_qkv_a, a_log_a, dt_bias_a, onorm_g_a, ssm_lambda_re, ssm_lambda_im, ssm_b_re, ssm_b_im, ssm_c_re, ssm_c_im, ssm_d, ssm_log_dt, w_glu_b, b_glu_b, w_out_ab, pool_w, pool_scale, xa_wq, xa_wkv, xa_wo, ffn_w_up, ffn_conv, ffn_w_down, loss_target, m_norm_mix_g, m_norm_xa_g, m_norm_ffn_g, m_norm_mem_g, m_norm_final_g, m_w_in_ab, m_conv_qkv_a, m_a_log_a, m_dt_bias_a, m_onorm_g_a, m_ssm_lambda_re, m_ssm_lambda_im, m_ssm_b_re, m_ssm_b_im, m_ssm_c_re, m_ssm_c_im, m_ssm_d, m_ssm_log_dt, m_w_glu_b, m_b_glu_b, m_w_out_ab, m_pool_w, m_pool_scale, m_xa_wq, m_xa_wkv, m_xa_wo, m_ffn_w_up, m_ffn_conv, m_ffn_w_down, v_norm_mix_g, v_norm_xa_g, v_norm_ffn_g, v_norm_mem_g, v_norm_final_g, v_w_in_ab, v_conv_qkv_a, v_a_log_a, v_dt_bias_a, v_onorm_g_a, v_ssm_lambda_re, v_ssm_lambda_im, v_ssm_b_re, v_ssm_b_im, v_ssm_c_re, v_ssm_c_im, v_ssm_d, v_ssm_log_dt, v_w_glu_b, v_b_glu_b, v_w_out_ab, v_pool_w, v_pool_scale, v_xa_wq, v_xa_wkv, v_xa_wo, v_ffn_w_up, v_ffn_conv, v_ffn_w_down):
    args = (x, mem, norm_mix_g, norm_xa_g, norm_ffn_g, norm_mem_g, norm_final_g, w_in_ab, conv_qkv_a, a_log_a, dt_bias_a, onorm_g_a, ssm_lambda_re, ssm_lambda_im, ssm_b_re, ssm_b_im, ssm_c_re, ssm_c_im, ssm_d, ssm_log_dt, w_glu_b, b_glu_b, w_out_ab, pool_w, pool_scale, xa_wq, xa_wkv, xa_wo, ffn_w_up, ffn_conv, ffn_w_down, loss_target, m_norm_mix_g, m_norm_xa_g, m_norm_ffn_g, m_norm_mem_g, m_norm_final_g, m_w_in_ab, m_conv_qkv_a, m_a_log_a, m_dt_bias_a, m_onorm_g_a, m_ssm_lambda_re, m_ssm_lambda_im, m_ssm_b_re, m_ssm_b_im, m_ssm_c_re, m_ssm_c_im, m_ssm_d, m_ssm_log_dt, m_w_glu_b, m_b_glu_b, m_w_out_ab, m_pool_w, m_pool_scale, m_xa_wq, m_xa_wkv, m_xa_wo, m_ffn_w_up, m_ffn_conv, m_ffn_w_down, v_norm_mix_g, v_norm_xa_g, v_norm_ffn_g, v_norm_mem_g, v_norm_final_g, v_w_in_ab, v_conv_qkv_a, v_a_log_a, v_dt_bias_a, v_onorm_g_a, v_ssm_lambda_re, v_ssm_lambda_im, v_ssm_b_re, v_ssm_b_im, v_ssm_c_re, v_ssm_c_im, v_ssm_d, v_ssm_log_dt, v_w_glu_b, v_b_glu_b, v_w_out_ab, v_pool_w, v_pool_scale, v_xa_wq, v_xa_wkv, v_xa_wo, v_ffn_w_up, v_ffn_conv, v_ffn_w_down)
    return _step(dict(zip(INPUT_NAMES, args)))
```

```python
import functools
import math

import numpy as np
import jax
import jax.numpy as jnp
from jax import lax
from jax.experimental import pallas as pl
from jax.experimental.pallas import tpu as pltpu
from jax.experimental.pallas import tpu_sc as plsc

F32, BF16 = jnp.float32, jnp.bfloat16
HIGH, HIGHEST = lax.Precision.HIGH, lax.Precision.HIGHEST
MESH = pl.DeviceIdType.MESH

N_DEV = 8
SEQ, D_MODEL, MEM_LEN = 2048, 1024, 256
WIDTH_A, N_HEADS_A, HEAD_A, CONV_A = 512, 4, 128, 4
GDR_CHUNK = 128
SSM_WIDTH, SSM_GROUP, N_GROUPS, SSM_STATE = 512, 16, 32, 64
SSM_CH = N_GROUPS * SSM_STATE
SCAN_CB = 512
POOL_WINDOWS = (2, 4, 8, 16)
POOL_GROUP = 256
N_HEADS_X, HEAD_X = 4, 256
D_FF, CONV_FFN = 2816, 3
RMS_EPS = 1e-6
ADAM_LR, ADAM_B1, ADAM_B2, ADAM_EPS, ADAM_WD, ADAM_STEP = 0.001, 0.9, 0.999, 1e-08, 0.01, 10
LANE = 128
PACK_COLS = 1024
VMEM_LIMIT_BYTES = 56 * 1024 * 1024


def _params(sem=None):
    return pltpu.CompilerParams(dimension_semantics=sem, vmem_limit_bytes=VMEM_LIMIT_BYTES)


class Comm:
    def __init__(self, inputs, out_shapes, sems, start, end, mid=None):
        self.inputs, self.out_shapes, self.sems = list(inputs), list(out_shapes), list(sems)
        self.start, self.mid, self.end = start, mid, end


def _merge_comms(comms):
    comms = [c for c in comms if c is not None]
    if not comms:
        return None, []
    bounds, ni, no, ns = [], 0, 0, 0
    for c in comms:
        bounds.append((ni, no, ns))
        ni, no, ns = ni + len(c.inputs), no + len(c.out_shapes), ns + len(c.sems)

    def phase(which):
        def run(ins, outs, sems):
            for c, (i0, o0, s0) in zip(comms, bounds):
                fn = getattr(c, which)
                if fn is not None:
                    fn(ins[i0:i0 + len(c.inputs)], outs[o0:o0 + len(c.out_shapes)], sems[s0:s0 + len(c.sems)])
        return run

    merged = Comm([a for c in comms for a in c.inputs], [s for c in comms for s in c.out_shapes],
                  [s for c in comms for s in c.sems], phase("start"), phase("end"), phase("mid"))
    return merged, [(o0, o0 + len(c.out_shapes)) for c, (_, o0, _) in zip(comms, bounds)]


def _call(body, *, name, grid, in_specs, out_specs, out_shape, args, scratch_shapes=(), sem=None, comm=None):
    single = not isinstance(out_shape, (list, tuple))
    out_specs_l = [out_specs] if single else list(out_specs)
    out_shape_l = [out_shape] if single else list(out_shape)
    scratch_shapes = list(scratch_shapes)
    merged, spans = _merge_comms(comm if isinstance(comm, (list, tuple)) else [comm])
    if merged is None:
        outs = pl.pallas_call(body, name=name, grid=grid, in_specs=list(in_specs), out_specs=out_specs_l,
                              out_shape=out_shape_l, scratch_shapes=scratch_shapes, compiler_params=_params(sem))(*args)
        outs = outs[0] if single else outs
        return outs if comm is None else (outs, [])
    n_in, n_out, n_scr = len(in_specs), len(out_specs_l), len(scratch_shapes)
    ci, co = len(merged.inputs), len(merged.out_shapes)
    total = int(np.prod(grid))

    def wrapped(*refs):
        ins, cins = refs[:n_in], refs[n_in:n_in + ci]
        outs, couts = refs[n_in + ci:n_in + ci + n_out], refs[n_in + ci + n_out:n_in + ci + n_out + co]
        scr, csems = refs[n_in + ci + n_out + co:n_in + ci + n_out + co + n_scr], refs[n_in + ci + n_out + co + n_scr:]
        lin = pl.program_id(0)
        for d in range(1, len(grid)):
            lin = lin * grid[d] + pl.program_id(d)
        pl.when(lin == 0)(lambda: merged.start(cins, couts, csems))
        body(*ins, *outs, *scr)
        mid_step = min((3 * total) // 4, total - 1)
        pl.when(lin == mid_step)(lambda: merged.mid(cins, couts, csems))
        pl.when(lin == total - 1)(lambda: merged.end(cins, couts, csems))

    any_spec = pl.BlockSpec(memory_space=pl.ANY)
    res = pl.pallas_call(
        wrapped, name=name, grid=grid, in_specs=list(in_specs) + [any_spec] * ci,
        out_specs=out_specs_l + [any_spec] * co, out_shape=out_shape_l + merged.out_shapes,
        scratch_shapes=scratch_shapes + merged.sems,
        compiler_params=_params(("arbitrary",) * len(grid)))(*args, *merged.inputs)
    outs, couts = res[:n_out], res[n_out:]
    return (outs[0] if single else list(outs)), [list(couts[a:b]) for a, b in spans]


def _comm_only(comm, name):
    def body():
        pass

    _, couts = _call(body, name=name, grid=(1,), in_specs=[], out_specs=[], out_shape=[], args=[], comm=comm)
    return couts[0]


def _tile(dim, pref):
    best = None
    for t in range(LANE, min(dim, pref) + 1, LANE):
        if dim % t == 0:
            best = t
    return best if best is not None else dim


MM_VMEM_BUDGET = 40 * 1024 * 1024


def _mm_tiles(m, n, k, a_bytes, b_bytes, o_bytes, r_bytes):
    for tk in (k, _tile(k, 2048), _tile(k, 1024), _tile(k, 512)):
        for tm, tn in ((1024, 1536), (1024, 1024), (1024, 512), (512, 512), (256, 512), (256, 256)):
            tm, tn = _tile(m, tm), _tile(n, tn)
            acc = 0 if tk == k else tm * tn * 4
            need = 2 * (tm * tk * a_bytes + tk * tn * b_bytes + tm * tn * (o_bytes + r_bytes)) + acc
            if need <= MM_VMEM_BUDGET:
                return tm, tn, tk
    raise ValueError("no matmul tiling fits VMEM")


def _mm(a, b, mode, name, out_dtype=F32, res=None, comm=None):
    if mode == "nn":
        (m, k), n = a.shape, b.shape[1]
    elif mode == "nt":
        (m, k), n = a.shape, b.shape[0]
    else:
        (k, m), n = a.shape, b.shape[1]
    tm, tn, tk = _mm_tiles(m, n, k, a.dtype.itemsize, b.dtype.itemsize, jnp.dtype(out_dtype).itemsize,
                           0 if res is None else res.dtype.itemsize)
    nk = k // tk
    dims = {"nn": ((1,), (0,)), "nt": ((1,), (1,)), "tn": ((0,), (0,))}[mode]

    def body(*refs):
        if res is None:
            a_ref, b_ref, o_ref = refs[:3]
            r_ref = None
        else:
            a_ref, b_ref, r_ref, o_ref = refs[:4]
        part = lax.dot_general(a_ref[...].astype(BF16), b_ref[...].astype(BF16), (dims, ((), ())),
                               preferred_element_type=F32)

        def finish(out):
            if r_ref is not None:
                out = out + r_ref[...].astype(F32)
            o_ref[...] = out.astype(out_dtype)

        if nk == 1:
            finish(part)
            return
        acc = refs[-1]
        kk = pl.program_id(2)

        @pl.when(kk == 0)
        def _():
            acc[...] = part

        @pl.when(kk > 0)
        def _():
            acc[...] += part

        @pl.when(kk == nk - 1)
        def _():
            finish(acc[...])

    a_spec = (pl.BlockSpec((tk, tm), lambda i, j, q: (q, i)) if mode == "tn"
              else pl.BlockSpec((tm, tk), lambda i, j, q: (i, q)))
    b_spec = (pl.BlockSpec((tn, tk), lambda i, j, q: (j, q)) if mode == "nt"
              else pl.BlockSpec((tk, tn), lambda i, j, q: (q, j)))
    o_spec = pl.BlockSpec((tm, tn), lambda i, j, q: (i, j))
    in_specs, args = [a_spec, b_spec], [a, b]
    if res is not None:
        in_specs.append(o_spec)
        args.append(res)
    return _call(body, name=name, grid=(m // tm, n // tn, nk), in_specs=in_specs, out_specs=o_spec,
                 out_shape=jax.ShapeDtypeStruct((m, n), out_dtype),
                 scratch_shapes=[] if nk == 1 else [pltpu.VMEM((tm, tn), F32)],
                 sem=("parallel", "parallel", "arbitrary"), args=args, comm=comm)


def _mm_parts(parts, b, mode, name, out_dtype=F32):
    count = len(parts)
    rows, cols = parts[0].shape
    n = b.shape[1]
    if mode == "nn":
        tm, tn, tk = _mm_tiles(rows, n, count * cols, parts[0].dtype.itemsize, b.dtype.itemsize,
                               jnp.dtype(out_dtype).itemsize, 0)
        assert tk == count * cols

        def body(*refs):
            a_refs, b_refs, o_ref = refs[:count], refs[count:2 * count], refs[2 * count]
            out = None
            for a_ref, b_ref in zip(a_refs, b_refs):
                part = jnp.dot(a_ref[...].astype(BF16), b_ref[...].astype(BF16), preferred_element_type=F32)
                out = part if out is None else out + part
            o_ref[...] = out.astype(out_dtype)

        return _call(body, name=name, grid=(rows // tm, n // tn),
                     in_specs=[pl.BlockSpec((tm, cols), lambda i, j: (i, 0))] * count
                     + [pl.BlockSpec((cols, tn), functools.partial(lambda q, i, j: (q, j), q)) for q in range(count)],
                     out_specs=pl.BlockSpec((tm, tn), lambda i, j: (i, j)),
                     out_shape=jax.ShapeDtypeStruct((rows, n), out_dtype),
                     sem=("parallel", "parallel"), args=(*parts, *([b] * count)))
    tm, tn = _tile(cols, 1536), _tile(n, 1024)
    per = cols // tm

    def body_t(*refs):
        a_refs, b_ref, o_ref = refs[:count], refs[count], refs[count + 1]
        for q, a_ref in enumerate(a_refs):
            @pl.when(pl.program_id(0) // per == q)
            def _(a_ref=a_ref):
                o_ref[...] = lax.dot_general(a_ref[...].astype(BF16), b_ref[...].astype(BF16),
                                             (((0,), (0,)), ((), ())), preferred_element_type=F32).astype(out_dtype)

    return _call(body_t, name=name, grid=(count * per, n // tn),
                 in_specs=[pl.BlockSpec((rows, tm), functools.partial(lambda q, i, j: (0, jnp.clip(i - q * per, 0, per - 1)), q))
                           for q in range(count)] + [pl.BlockSpec((rows, tn), lambda i, j: (0, j))],
                 out_specs=pl.BlockSpec((tm, tn), lambda i, j: (i, j)),
                 out_shape=jax.ShapeDtypeStruct((count * cols, n), out_dtype),
                 sem=("parallel", "parallel"), args=(*parts, b))


def _mm_bd(a, b, mode, name, out_dtype=F32, res=None, comm=None, tm=1024):
    if mode == "tn":
        k = a.shape[0]
        nb = min(a.shape[1], b.shape[1]) // LANE
        ma, n = a.shape[1] // nb, b.shape[1] // nb

        def body(a_ref, b_ref, o_ref):
            o_ref[0] = lax.dot_general(a_ref[...].astype(BF16), b_ref[...].astype(BF16), (((0,), (0,)), ((), ())),
                                       preferred_element_type=F32).astype(out_dtype)

        return _call(body, name=name, grid=(nb,),
                     in_specs=[pl.BlockSpec((k, ma), lambda j: (0, j)), pl.BlockSpec((k, n), lambda j: (0, j))],
                     out_specs=pl.BlockSpec((1, ma, n), lambda j: (j, 0, 0)),
                     out_shape=jax.ShapeDtypeStruct((nb, ma, n), out_dtype), sem=("parallel",), args=(a, b), comm=comm)
    m = a.shape[0]
    nb = b.shape[0]
    ka = a.shape[1] // nb
    n = b.shape[2] if mode == "nn" else b.shape[1]
    tm = _tile(m, tm)
    dims = ((1,), (0,)) if mode == "nn" else ((1,), (1,))

    def body(*refs):
        if res is None:
            a_ref, b_ref, o_ref = refs
            r_ref = None
        else:
            a_ref, b_ref, r_ref, o_ref = refs
        out = lax.dot_general(a_ref[...].astype(BF16), b_ref[0].astype(BF16), (dims, ((), ())),
                              preferred_element_type=F32)
        if r_ref is not None:
            out = out + r_ref[...].astype(F32)
        o_ref[...] = out.astype(out_dtype)

    o_spec = pl.BlockSpec((tm, n), lambda i, j: (i, j))
    in_specs = [pl.BlockSpec((tm, ka), lambda i, j: (i, j)), pl.BlockSpec((1,) + b.shape[1:], lambda i, j: (j, 0, 0))]
    args = [a, b]
    if res is not None:
        in_specs.append(o_spec)
        args.append(res)
    return _call(body, name=name, grid=(m // tm, nb), in_specs=in_specs, out_specs=o_spec,
                 out_shape=jax.ShapeDtypeStruct((m, nb * n), out_dtype), sem=("parallel", "parallel"),
                 args=args, comm=comm)


def _rms_fwd(x, g, out_dtype, name, tr=512):
    rows, d = x.shape
    tr = min(tr, rows)

    def body(x_ref, g_ref, o_ref):
        xv = x_ref[...]
        r = lax.rsqrt(jnp.mean(xv * xv, axis=-1, keepdims=True) + RMS_EPS)
        o_ref[...] = (xv * r * g_ref[...]).astype(out_dtype)

    return pl.pallas_call(
        body, name=name, grid=(rows // tr,),
        in_specs=[pl.BlockSpec((tr, d), lambda i: (i, 0)), pl.BlockSpec((1, d), lambda i: (0, 0))],
        out_specs=pl.BlockSpec((tr, d), lambda i: (i, 0)), out_shape=jax.ShapeDtypeStruct((rows, d), out_dtype),
        compiler_params=_params(("parallel",)))(x, g.reshape(1, d))


def _rms_bwd(x, g, dy, dres, name, tr=512, comm=None):
    rows, d = x.shape
    tr = min(tr, rows)

    def body(*refs):
        if dres is None:
            x_ref, g_ref, dy_ref, dx_ref, dg_ref = refs
            r_ref = None
        else:
            x_ref, g_ref, dy_ref, r_ref, dx_ref, dg_ref = refs

        @pl.when(pl.program_id(0) == 0)
        def _():
            dg_ref[...] = jnp.zeros_like(dg_ref)

        xv, dyv = x_ref[...], dy_ref[...].astype(F32)
        r = lax.rsqrt(jnp.mean(xv * xv, axis=-1, keepdims=True) + RMS_EPS)
        xh = xv * r
        dyg = dyv * g_ref[...]
        dx = r * (dyg - xh * jnp.mean(dyg * xh, axis=-1, keepdims=True))
        if r_ref is not None:
            dx = dx + r_ref[...]
        dx_ref[...] = dx
        dg_ref[...] += jnp.sum(dyv * xh, axis=0, keepdims=True)

    blk = pl.BlockSpec((tr, d), lambda i: (i, 0))
    vec = pl.BlockSpec((1, d), lambda i: (0, 0))
    in_specs, args = [blk, vec, blk], [x, g.reshape(1, d), dy]
    if dres is not None:
        in_specs.append(blk)
        args.append(dres)
    return _call(
        body, name=name, grid=(rows // tr,), in_specs=in_specs, out_specs=[blk, vec],
        out_shape=[jax.ShapeDtypeStruct((rows, d), F32), jax.ShapeDtypeStruct((1, d), F32)],
        sem=("arbitrary",), args=args, comm=comm)


def _loss_head(x, g, target, name, tr=256):
    rows, d = x.shape

    def body(x_ref, g_ref, t_ref, loss_ref, dx_ref, dg_ref):
        @pl.when(pl.program_id(0) == 0)
        def _():
            dg_ref[...] = jnp.zeros_like(dg_ref)
            loss_ref[...] = jnp.zeros_like(loss_ref)

        xv = x_ref[...]
        r = lax.rsqrt(jnp.mean(xv * xv, axis=-1, keepdims=True) + RMS_EPS)
        xh = xv * r
        err = xh * g_ref[...] - t_ref[...]
        loss_ref[...] += 0.5 * jnp.sum(jnp.mean(err * err, axis=-1, keepdims=True), keepdims=True)
        dyv = err * (1.0 / d)
        dyg = dyv * g_ref[...]
        dx_ref[...] = r * (dyg - xh * jnp.mean(dyg * xh, axis=-1, keepdims=True))
        dg_ref[...] += jnp.sum(dyv * xh, axis=0, keepdims=True)

    blk = pl.BlockSpec((tr, d), lambda i: (i, 0))
    vec = pl.BlockSpec((1, d), lambda i: (0, 0))
    return pl.pallas_call(
        body, name=name, grid=(rows // tr,), in_specs=[blk, vec, blk],
        out_specs=[pl.BlockSpec((1, 1), lambda i: (0, 0)), blk, vec],
        out_shape=[jax.ShapeDtypeStruct((1, 1), F32), jax.ShapeDtypeStruct((rows, d), F32),
                   jax.ShapeDtypeStruct((1, d), F32)],
        compiler_params=_params(("arbitrary",)))(x, g.reshape(1, d), target)


def _shift_down(x, s):
    rows = lax.broadcasted_iota(jnp.int32, x.shape, 0)
    return jnp.where(rows >= s, pltpu.roll(x, s, 0), 0.0)


def _shift_up(x, s):
    n = x.shape[0]
    rows = lax.broadcasted_iota(jnp.int32, x.shape, 0)
    return jnp.where(rows < n - s, pltpu.roll(x, n - s, 0), 0.0)


def _sigmoid(x):
    return 1.0 / (1.0 + jnp.exp(-x))


def _silu_and_grad(x):
    s = _sigmoid(x)
    return x * s, s * (1.0 + x * (1.0 - s))


_GELU_C0, _GELU_C1 = math.sqrt(2.0 / math.pi), 0.044715


def _gelu_and_grad(x):
    th = jnp.tanh(_GELU_C0 * (x + _GELU_C1 * x * x * x))
    y = 0.5 * x * (1.0 + th)
    dy = 0.5 * (1.0 + th) + 0.5 * x * (1.0 - th * th) * _GELU_C0 * (1.0 + 3.0 * _GELU_C1 * x * x)
    return y, dy


def _ffn_act_fwd(h, w, name, tc=256, comm=None):
    t = h.shape[0]
    nb = D_FF // tc

    def body(hg_ref, hv_ref, wg_ref, wv_ref, a_ref):
        def conv(x, wr):
            return wr[2:3, :] * x + wr[1:2, :] * _shift_down(x, 1) + wr[0:1, :] * _shift_down(x, 2)

        cg = conv(hg_ref[...], wg_ref[...])
        cv = conv(hv_ref[...], wv_ref[...])
        a_ref[...] = (cg * _sigmoid(cg) * cv).astype(BF16)

    return _call(
        body, name=name, grid=(nb,),
        in_specs=[pl.BlockSpec((t, tc), lambda j: (0, j)), pl.BlockSpec((t, tc), lambda j: (0, j + nb)),
                  pl.BlockSpec((CONV_FFN, tc), lambda j: (0, j)), pl.BlockSpec((CONV_FFN, tc), lambda j: (0, j + nb))],
        out_specs=pl.BlockSpec((t, tc), lambda j: (0, j)), out_shape=jax.ShapeDtypeStruct((t, D_FF), BF16),
        sem=("parallel",), args=(h, h, w, w), comm=comm)


def _ffn_act_bwd(h, w, da, name, tc=256, comm=None):
    t = h.shape[0]
    nb = D_FF // tc

    def body(hg_ref, hv_ref, wg_ref, wv_ref, da_ref, dhg_ref, dhv_ref, dwg_ref, dwv_ref):
        hg, hv, wg, wv = hg_ref[...], hv_ref[...], wg_ref[...], wv_ref[...]
        hg1, hg2, hv1, hv2 = _shift_down(hg, 1), _shift_down(hg, 2), _shift_down(hv, 1), _shift_down(hv, 2)
        cg = wg[2:3, :] * hg + wg[1:2, :] * hg1 + wg[0:1, :] * hg2
        cv = wv[2:3, :] * hv + wv[1:2, :] * hv1 + wv[0:1, :] * hv2
        sg, dsg = _silu_and_grad(cg)
        dav = da_ref[...].astype(F32)
        dcv = dav * sg
        dcg = dav * cv * dsg

        def conv_t(dc, wr):
            return wr[2:3, :] * dc + wr[1:2, :] * _shift_up(dc, 1) + wr[0:1, :] * _shift_up(dc, 2)

        dhg_ref[...] = conv_t(dcg, wg).astype(BF16)
        dhv_ref[...] = conv_t(dcv, wv).astype(BF16)
        dwg_ref[0:1, :] = jnp.sum(dcg * hg2, axis=0, keepdims=True)
        dwg_ref[1:2, :] = jnp.sum(dcg * hg1, axis=0, keepdims=True)
        dwg_ref[2:3, :] = jnp.sum(dcg * hg, axis=0, keepdims=True)
        dwv_ref[0:1, :] = jnp.sum(dcv * hv2, axis=0, keepdims=True)
        dwv_ref[1:2, :] = jnp.sum(dcv * hv1, axis=0, keepdims=True)
        dwv_ref[2:3, :] = jnp.sum(dcv * hv, axis=0, keepdims=True)

    big = lambda off: pl.BlockSpec((t, tc), lambda j: (0, j + off))
    small = lambda off: pl.BlockSpec((CONV_FFN, tc), lambda j: (0, j + off))
    res = _call(
        body, name=name, grid=(nb,),
        in_specs=[big(0), big(nb), small(0), small(nb), big(0)],
        out_specs=[big(0), big(0), small(0), small(0)],
        out_shape=[jax.ShapeDtypeStruct((t, D_FF), BF16), jax.ShapeDtypeStruct((t, D_FF), BF16),
                   jax.ShapeDtypeStruct((CONV_FFN, D_FF), F32), jax.ShapeDtypeStruct((CONV_FFN, D_FF), F32)],
        sem=("parallel",), args=(h, h, w, w, da), comm=comm)
    (dhg, dhv, dwg, dwv), couts = res if comm is not None else (res, None)
    out = ((dhg, dhv), jnp.concatenate([dwg, dwv], axis=1))
    return out if comm is None else (out, couts)


def _attn_probs(q, k):
    s = lax.dot_general(q.astype(BF16), k.astype(BF16), (((1,), (1,)), ((), ())),
                        preferred_element_type=F32) * (HEAD_X ** -0.5)
    s = s - jnp.max(s, axis=-1, keepdims=True)
    p = jnp.exp(s)
    return p / jnp.sum(p, axis=-1, keepdims=True)


def _attn_fwd(q, kv, name, tq=512, comm=None):
    t = q.shape[0]

    def body(q_ref, k_ref, v_ref, o_ref):
        p = _attn_probs(q_ref[...], k_ref[...])
        o_ref[...] = jnp.dot(p.astype(BF16), v_ref[...].astype(BF16), preferred_element_type=F32).astype(BF16)

    return _call(
        body, name=name, grid=(N_HEADS_X, t // tq),
        in_specs=[pl.BlockSpec((tq, HEAD_X), lambda h, i: (i, h)),
                  pl.BlockSpec((MEM_LEN, HEAD_X), lambda h, i: (0, h)),
                  pl.BlockSpec((MEM_LEN, HEAD_X), lambda h, i: (0, h + N_HEADS_X))],
        out_specs=pl.BlockSpec((tq, HEAD_X), lambda h, i: (i, h)),
        out_shape=jax.ShapeDtypeStruct((t, N_HEADS_X * HEAD_X), BF16),
        sem=("parallel", "parallel"), args=(q, kv, kv), comm=comm)


def _attn_bwd(q, kv, do, name, tq=512):
    t = q.shape[0]

    def body(q_ref, k_ref, v_ref, do_ref, dq_ref, dk_ref, dv_ref):
        @pl.when(pl.program_id(1) == 0)
        def _():
            dk_ref[...] = jnp.zeros_like(dk_ref)
            dv_ref[...] = jnp.zeros_like(dv_ref)

        qb, kb, vb, dob = (r[...].astype(BF16) for r in (q_ref, k_ref, v_ref, do_ref))
        p = _attn_probs(qb, kb)
        dp = lax.dot_general(dob, vb, (((1,), (1,)), ((), ())), preferred_element_type=F32)
        ds = p * (dp - jnp.sum(dp * p, axis=-1, keepdims=True)) * (HEAD_X ** -0.5)
        dsb = ds.astype(BF16)
        dq_ref[...] = jnp.dot(dsb, kb, preferred_element_type=F32).astype(BF16)
        dk_ref[...] += lax.dot_general(dsb, qb, (((0,), (0,)), ((), ())), preferred_element_type=F32)
        dv_ref[...] += lax.dot_general(p.astype(BF16), dob, (((0,), (0,)), ((), ())), preferred_element_type=F32)

    qs = pl.BlockSpec((tq, HEAD_X), lambda h, i: (i, h))
    ms = pl.BlockSpec((MEM_LEN, HEAD_X), lambda h, i: (0, h))
    return pl.pallas_call(
        body, name=name, grid=(N_HEADS_X, t // tq),
        in_specs=[qs, ms, pl.BlockSpec((MEM_LEN, HEAD_X), lambda h, i: (0, h + N_HEADS_X)), qs],
        out_specs=[qs, ms, ms],
        out_shape=[jax.ShapeDtypeStruct((t, D_MODEL), BF16), jax.ShapeDtypeStruct((MEM_LEN, D_MODEL), F32),
                   jax.ShapeDtypeStruct((MEM_LEN, D_MODEL), F32)],
        compiler_params=_params(("parallel", "arbitrary")))(q, kv, kv, do)


def _pool_counts(t, win):
    pos = lax.broadcasted_iota(jnp.int32, (t, 1), 0).astype(F32) + 1.0
    return 1.0 / jnp.minimum(pos, float(win))


def _pool_delta(xv, win):
    s, step = xv, 1
    while step < win:
        s = s + _shift_down(s, step)
        step *= 2
    return s * _pool_counts(xv.shape[0], win) - xv


def _pool_delta_t(dv, win):
    s, step = dv * _pool_counts(dv.shape[0], win), 1
    while step < win:
        s = s + _shift_up(s, step)
        step *= 2
    return s - dv


def _pool_fwd(xn, w, scale, res, name):
    t = xn.shape[0]

    def make_branch(win, xn_ref, w_ref, s_ref, r_ref, o_ref):
        def branch():
            dl = _pool_delta(xn_ref[...], win)
            y = jnp.dot(dl.astype(BF16), w_ref[0], preferred_element_type=F32)
            o_ref[...] = r_ref[...] + y * s_ref[...]
        return branch

    def body(xn_ref, w_ref, s_ref, r_ref, o_ref):
        for gi, win in enumerate(POOL_WINDOWS):
            pl.when(pl.program_id(0) == gi)(make_branch(win, xn_ref, w_ref, s_ref, r_ref, o_ref))

    blk = pl.BlockSpec((t, POOL_GROUP), lambda g: (0, g))
    return pl.pallas_call(
        body, name=name, grid=(len(POOL_WINDOWS),),
        in_specs=[blk, pl.BlockSpec((1, POOL_GROUP, POOL_GROUP), lambda g: (g, 0, 0)),
                  pl.BlockSpec((1, POOL_GROUP), lambda g: (0, g)), blk],
        out_specs=blk, out_shape=jax.ShapeDtypeStruct((t, D_MODEL), F32),
        compiler_params=_params(("parallel",)))(xn, w, scale, res)


def _pool_bwd(xn, w, scale, dmix, name):
    t = xn.shape[0]

    def make_branch(win, xn_ref, w_ref, s_ref, d_ref, dxn_ref, dw_ref, ds_ref):
        def branch():
            dl = _pool_delta(xn_ref[...], win).astype(BF16)
            wv = w_ref[0]
            dm = d_ref[...]
            y = jnp.dot(dl, wv, preferred_element_type=F32)
            ds_ref[...] = jnp.sum(dm * y, axis=0, keepdims=True)
            dy = (dm * s_ref[...]).astype(BF16)
            dw_ref[0] = lax.dot_general(dl, dy, (((0,), (0,)), ((), ())), preferred_element_type=F32)
            ddl = lax.dot_general(dy, wv, (((1,), (1,)), ((), ())), preferred_element_type=F32)
            dxn_ref[...] = _pool_delta_t(ddl, win)
        return branch

    def body(*refs):
        for gi, win in enumerate(POOL_WINDOWS):
            pl.when(pl.program_id(0) == gi)(make_branch(win, *refs))

    blk = pl.BlockSpec((t, POOL_GROUP), lambda g: (0, g))
    wspec = pl.BlockSpec((1, POOL_GROUP, POOL_GROUP), lambda g: (g, 0, 0))
    vec = pl.BlockSpec((1, POOL_GROUP), lambda g: (0, g))
    return pl.pallas_call(
        body, name=name, grid=(len(POOL_WINDOWS),), in_specs=[blk, wspec, vec, blk], out_specs=[blk, wspec, vec],
        out_shape=[jax.ShapeDtypeStruct((t, D_MODEL), F32),
                   jax.ShapeDtypeStruct((len(POOL_WINDOWS), POOL_GROUP, POOL_GROUP), F32),
                   jax.ShapeDtypeStruct((1, D_MODEL), F32)],
        compiler_params=_params(("parallel",)))(xn, w, scale, dmix)


def _qkv_conv(h, wr):
    return (wr[3:4, :] * h + wr[2:3, :] * _shift_down(h, 1) + wr[1:2, :] * _shift_down(h, 2)
            + wr[0:1, :] * _shift_down(h, 3))


def _qkv_block_kind(j):
    return j < 2 * N_HEADS_A, jnp.where(j < N_HEADS_A, HEAD_A ** -0.5, 1.0)


def _qkv_pre_fwd(h, w, name):
    t, cols = h.shape

    def body(h_ref, w_ref, o_ref):
        normalised, scale = _qkv_block_kind(pl.program_id(0))
        c = _qkv_conv(h_ref[...], w_ref[...])
        s = c * _sigmoid(c)
        r = lax.rsqrt(jnp.sum(s * s, axis=-1, keepdims=True) + 1e-6)
        o_ref[...] = jnp.where(normalised, s * (r * scale), s)

    blk = pl.BlockSpec((t, HEAD_A), lambda j: (0, j))
    return pl.pallas_call(
        body, name=name, grid=(cols // HEAD_A,), in_specs=[blk, pl.BlockSpec((CONV_A, HEAD_A), lambda j: (0, j))],
        out_specs=blk, out_shape=jax.ShapeDtypeStruct((t, cols), F32), compiler_params=_params(("parallel",)))(h, w)


def _qkv_pre_bwd(h, w, dy, name):
    t, cols = h.shape

    def body(h_ref, w_ref, dy_ref, dh_ref, dw_ref):
        normalised, scale = _qkv_block_kind(pl.program_id(0))
        hv, wr, dyv = h_ref[...], w_ref[...], dy_ref[...]
        h1, h2, h3 = _shift_down(hv, 1), _shift_down(hv, 2), _shift_down(hv, 3)
        c = wr[3:4, :] * hv + wr[2:3, :] * h1 + wr[1:2, :] * h2 + wr[0:1, :] * h3
        s, dsilu = _silu_and_grad(c)
        r = lax.rsqrt(jnp.sum(s * s, axis=-1, keepdims=True) + 1e-6)
        y = s * r
        dys = dyv * scale
        ds = jnp.where(normalised, r * (dys - y * jnp.sum(dys * y, axis=-1, keepdims=True)), dyv)
        dc = ds * dsilu
        dh = (wr[3:4, :] * dc + wr[2:3, :] * _shift_up(dc, 1) + wr[1:2, :] * _shift_up(dc, 2)
              + wr[0:1, :] * _shift_up(dc, 3))
        dh_ref[...] = dh.astype(BF16)
        dw_ref[0:1, :] = jnp.sum(dc * h3, axis=0, keepdims=True)
        dw_ref[1:2, :] = jnp.sum(dc * h2, axis=0, keepdims=True)
        dw_ref[2:3, :] = jnp.sum(dc * h1, axis=0, keepdims=True)
        dw_ref[3:4, :] = jnp.sum(dc * hv, axis=0, keepdims=True)

    blk = pl.BlockSpec((t, HEAD_A), lambda j: (0, j))
    taps = pl.BlockSpec((CONV_A, HEAD_A), lambda j: (0, j))
    return pl.pallas_call(
        body, name=name, grid=(cols // HEAD_A,), in_specs=[blk, taps, blk], out_specs=[blk, taps],
        out_shape=[jax.ShapeDtypeStruct((t, cols), BF16), jax.ShapeDtypeStruct((CONV_A, cols), F32)],
        compiler_params=_params(("parallel",)))(h, w, dy)


def _softplus(x):
    return jnp.maximum(x, 0.0) + jnp.log1p(jnp.exp(-jnp.abs(x)))


def _gates_fwd(ba, arow, brow, name):
    t = ba.shape[0]

    def body(x_ref, a_ref, b_ref, o_ref):
        xv = x_ref[...]
        lane = lax.broadcasted_iota(jnp.int32, xv.shape, 1)
        beta = _sigmoid(xv)
        g = -jnp.exp(a_ref[...]) * _softplus(xv + b_ref[...])
        o_ref[...] = jnp.where(lane < N_HEADS_A, beta, jnp.where(lane < 2 * N_HEADS_A, g, 0.0))

    return pl.pallas_call(body, name=name, out_shape=jax.ShapeDtypeStruct((t, LANE), F32),
                          compiler_params=_params())(ba, arow, brow)


def _gates_bwd(ba, arow, brow, dgb, name):
    t = ba.shape[0]

    def body(x_ref, a_ref, b_ref, d_ref, dx_ref, da_ref, db_ref):
        xv = x_ref[...]
        dv = d_ref[0] + d_ref[1] + d_ref[2] + d_ref[3]
        lane = lax.broadcasted_iota(jnp.int32, xv.shape, 1)
        beta = _sigmoid(xv)
        ea = jnp.exp(a_ref[...])
        z = xv + b_ref[...]
        dgv = jnp.where((lane >= N_HEADS_A) & (lane < 2 * N_HEADS_A), dv, 0.0) * (-ea)
        dz = dgv * _sigmoid(z)
        dx = jnp.where(lane < N_HEADS_A, dv * beta * (1.0 - beta), dz)
        dx_ref[...] = dx.astype(BF16)
        db_ref[...] = jnp.sum(dz, axis=0, keepdims=True)
        da_ref[...] = jnp.sum(dgv * _softplus(z), axis=0, keepdims=True)

    return pl.pallas_call(
        body, name=name,
        out_shape=[jax.ShapeDtypeStruct((t, LANE), BF16), jax.ShapeDtypeStruct((1, LANE), F32),
                   jax.ShapeDtypeStruct((1, LANE), F32)],
        compiler_params=_params())(ba, arow, brow, dgb)


def _head_gates(gates, head):
    lane = lax.broadcasted_iota(jnp.int32, gates.shape, 1)
    beta = jnp.sum(jnp.where(lane == head, gates, 0.0), axis=1, keepdims=True)
    g = jnp.sum(jnp.where(lane == head + N_HEADS_A, gates, 0.0), axis=1, keepdims=True)
    return beta, g


_B_NN, _B_NT, _B_TN = ((2,), (1,)), ((2,), (2,)), ((1,), (1,))


def _bdot(a, b, dims=_B_NN, prec=None):
    if prec is None:
        a, b = a.astype(BF16), b.astype(BF16)
    return lax.dot_general(a, b, (dims, ((0,), (0,))), precision=prec, preferred_element_type=F32)


def _heads_of(ref):
    return jnp.stack([ref[:, h * HEAD_A:(h + 1) * HEAD_A] for h in range(N_HEADS_A)])


def _all_head_gates(gates):
    pairs = [_head_gates(gates, h) for h in range(N_HEADS_A)]
    return jnp.stack([b for b, _ in pairs]), jnp.stack([g for _, g in pairs])


def _gdr_terms(k, beta, g):
    h, c = k.shape[0], GDR_CHUNK
    row = lax.broadcasted_iota(jnp.int32, (c, c), 0)
    col = lax.broadcasted_iota(jnp.int32, (c, c), 1)
    causal, strict = row >= col, row > col
    lower = jnp.broadcast_to(causal.astype(F32), (h, c, c))
    gcum = _bdot(lower, jnp.broadcast_to(g, (h, c, c)), prec=HIGHEST)
    diff = gcum - jnp.swapaxes(gcum, 1, 2)
    decay = jnp.where(causal, jnp.exp(jnp.where(causal, diff, 0.0)), 0.0)
    kb = k * beta
    return row, col, causal, strict, gcum, decay, kb, _bdot(kb, k, _B_NT)


def _unit_lower_inverses(a):
    c = a.shape[1]
    eye = (lax.broadcasted_iota(jnp.int32, (c, c), 0) == lax.broadcasted_iota(jnp.int32, (c, c), 1)).astype(F32)
    p = -a
    inv = eye + p
    step = 1
    while 2 * step < c:
        p = _bdot(p, p, prec=HIGH)
        inv = inv + _bdot(inv, p, prec=HIGH)
        step *= 2
    return inv


def _gdr_fwd(qkv, gates, name, comm=None):
    t = qkv.shape[0]
    c, nh = GDR_CHUNK, N_HEADS_A
    n = t // c

    def body(q_ref, k_ref, v_ref, gb_ref, o_ref, tm_ref, s_ref, state):
        @pl.when(pl.program_id(0) == 0)
        def _():
            state[...] = jnp.zeros_like(state)

        qv, kv, vv = _heads_of(q_ref), _heads_of(k_ref), _heads_of(v_ref)
        beta, g = _all_head_gates(gb_ref[...])
        row, col, causal, strict, gcum, decay, kb, kk = _gdr_terms(kv, beta, g)
        tm = _unit_lower_inverses(jnp.where(strict, kk * decay, 0.0))
        e = jnp.exp(gcum)
        u = _bdot(tm, vv * beta, prec=HIGH)
        w = _bdot(tm, kb * e, prec=HIGH)
        p = jnp.where(causal, _bdot(qv, kv, _B_NT) * decay, 0.0)
        s = state[...]
        s_ref[:, 0] = s
        tm_ref[:, 0] = tm
        vn = u - _bdot(w, s)
        o = _bdot(qv * e, s) + _bdot(p, vn)
        for h in range(nh):
            o_ref[:, h * HEAD_A:(h + 1) * HEAD_A] = o[h]
        glast = gcum[:, c - 1:c, :]
        state[...] = s * jnp.exp(glast) + _bdot(kv * jnp.exp(glast - gcum), vn, _B_TN)

    part = lambda p: pl.BlockSpec((c, WIDTH_A), lambda i: (i, p))
    mat = pl.BlockSpec((nh, 1, c, c), lambda i: (0, i, 0, 0))
    return _call(
        body, name=name, grid=(n,), in_specs=[part(0), part(1), part(2), pl.BlockSpec((c, LANE), lambda i: (i, 0))],
        out_specs=[part(0), mat, mat],
        out_shape=[jax.ShapeDtypeStruct((t, WIDTH_A), F32), jax.ShapeDtypeStruct((nh, n, c, c), F32),
                   jax.ShapeDtypeStruct((nh, n, HEAD_A, HEAD_A), F32)],
        scratch_shapes=[pltpu.VMEM((nh, HEAD_A, HEAD_A), F32)], sem=("arbitrary",),
        args=(qkv, qkv, qkv, gates), comm=comm)


def _gdr_bwd(qkv, gates, tm_all, s_all, do, name, comm=None):
    t = qkv.shape[0]
    c, nh = GDR_CHUNK, N_HEADS_A
    n = t // c

    def body(q_ref, k_ref, v_ref, gb_ref, tm_ref, s_ref, do_ref, dqkv_ref, dgb_ref, dstate):
        @pl.when(pl.program_id(0) == 0)
        def _():
            dstate[...] = jnp.zeros_like(dstate)

        qv, kv, vv, dov = _heads_of(q_ref), _heads_of(k_ref), _heads_of(v_ref), _heads_of(do_ref)
        beta, g = _all_head_gates(gb_ref[...])
        tm, s, dsp = tm_ref[:, 0], s_ref[:, 0], dstate[...]
        row, col, causal, strict, gcum, decay, kb, kk = _gdr_terms(kv, beta, g)
        rowsum = lambda x: jnp.sum(x, axis=2, keepdims=True)
        e = jnp.exp(gcum)
        vb, kbe = vv * beta, kb * e
        u = _bdot(tm, vb, prec=HIGH)
        w = _bdot(tm, kbe, prec=HIGH)
        qk = _bdot(qv, kv, _B_NT)
        p = jnp.where(causal, qk * decay, 0.0)
        vn = u - _bdot(w, s)
        glast = gcum[:, c - 1:c, :]
        el = jnp.exp(glast)
        f = jnp.exp(glast - gcum)
        kd = kv * f
        qe = qv * e

        dvn = _bdot(p, dov, _B_TN) + _bdot(kd, dsp)
        dglast = el[:, :, 0:1] * jnp.sum(s * dsp, axis=(1, 2), keepdims=True)
        dkd = _bdot(vn, dsp, _B_NT)
        dk = dkd * f
        df = rowsum(dkd * kv) * f[:, :, 0:1]
        dglast = dglast + jnp.sum(df, axis=1, keepdims=True)
        dgc = -df
        dp = jnp.where(causal, _bdot(dov, vn, _B_NT), 0.0)
        dqe = _bdot(dov, s, _B_NT)
        dq = dqe * e
        de = rowsum(dqe * qv)
        dstate[...] = dsp * el + _bdot(qe, dov, _B_TN) - _bdot(w, dvn, _B_TN)
        dw = -_bdot(dvn, s, _B_NT)
        dvb = _bdot(tm, dvn, _B_TN, prec=HIGH)
        dkbe = _bdot(tm, dw, _B_TN, prec=HIGH)
        da = -jnp.where(strict, _bdot(dvb, u, _B_NT) + _bdot(dkbe, w, _B_NT), 0.0)
        dkk = da * decay
        dqk = dp * decay
        dd = da * kk + dp * qk
        dq = dq + _bdot(dqk, kv)
        dk = dk + _bdot(dqk, qv, _B_TN)
        dkb = _bdot(dkk, kv) + dkbe * e
        dk = dk + _bdot(dkk, kb, _B_TN)
        de = de + rowsum(dkbe * kb)
        dk = dk + dkb * beta
        dbeta = rowsum(dkb * kv) + rowsum(dvb * vv)
        m = dd * decay
        dgc = dgc + rowsum(m) - rowsum(jnp.swapaxes(m, 1, 2))
        dgc = dgc + de * e[:, :, 0:1]
        dgc = dgc + jnp.where(row[:, 0:1] == c - 1, dglast, 0.0)
        upper = jnp.broadcast_to((row <= col).astype(F32), (nh, c, c))
        dg = _bdot(upper, jnp.broadcast_to(dgc, (nh, c, c)), prec=HIGHEST)
        dv = dvb * beta
        for p, grad in enumerate((dq, dk, dv)):
            for h in range(nh):
                dqkv_ref[:, p * WIDTH_A + h * HEAD_A:p * WIDTH_A + (h + 1) * HEAD_A] = grad[h]
        head = lax.broadcasted_iota(jnp.int32, (nh, c, LANE), 0)
        lane = lax.broadcasted_iota(jnp.int32, (nh, c, LANE), 2)
        dgb_ref[...] = jnp.where(lane == head, dbeta, jnp.where(lane == head + nh, dg, 0.0))

    part = lambda p: pl.BlockSpec((c, WIDTH_A), lambda i: (n - 1 - i, p))
    mat = pl.BlockSpec((nh, 1, c, c), lambda i: (0, n - 1 - i, 0, 0))
    return _call(
        body, name=name, grid=(n,),
        in_specs=[part(0), part(1), part(2), pl.BlockSpec((c, LANE), lambda i: (n - 1 - i, 0)), mat, mat, part(0)],
        out_specs=[pl.BlockSpec((c, 3 * WIDTH_A), lambda i: (n - 1 - i, 0)),
                   pl.BlockSpec((nh, c, LANE), lambda i: (0, n - 1 - i, 0))],
        out_shape=[jax.ShapeDtypeStruct((t, 3 * WIDTH_A), F32), jax.ShapeDtypeStruct((nh, t, LANE), F32)],
        scratch_shapes=[pltpu.VMEM((nh, HEAD_A, HEAD_A), F32)], sem=("arbitrary",),
        args=(qkv, qkv, qkv, gates, tm_all, s_all, do), comm=comm)


def _onorm_fwd(o, gate, g, name):
    t = o.shape[0]

    def body(o_ref, gate_ref, g_ref, y_ref):
        ov, gv = o_ref[...], gate_ref[...]
        r = lax.rsqrt(jnp.mean(ov * ov, axis=-1, keepdims=True) + RMS_EPS)
        y_ref[...] = (ov * r * g_ref[...] * gv * _sigmoid(gv)).astype(BF16)

    blk = pl.BlockSpec((t, HEAD_A), lambda j: (0, j))
    return pl.pallas_call(
        body, name=name, grid=(N_HEADS_A,), in_specs=[blk, blk, pl.BlockSpec((1, HEAD_A), lambda j: (0, 0))],
        out_specs=blk, out_shape=jax.ShapeDtypeStruct((t, WIDTH_A), BF16),
        compiler_params=_params(("parallel",)))(o, gate, g)


def _onorm_bwd(o, gate, g, dy, name):
    t = o.shape[0]

    def body(o_ref, gate_ref, g_ref, dy_ref, do_ref, dgate_ref, dg_ref):
        @pl.when(pl.program_id(0) == 0)
        def _():
            dg_ref[...] = jnp.zeros_like(dg_ref)

        ov, gv, dyv = o_ref[...], gate_ref[...], dy_ref[...].astype(F32)
        r = lax.rsqrt(jnp.mean(ov * ov, axis=-1, keepdims=True) + RMS_EPS)
        oh = ov * r
        sg, dsg = _silu_and_grad(gv)
        dgate_ref[...] = (dyv * oh * g_ref[...] * dsg).astype(BF16)
        dn = dyv * sg
        dg_ref[...] += jnp.sum(dn * oh, axis=0, keepdims=True)
        dng = dn * g_ref[...]
        do_ref[...] = r * (dng - oh * jnp.mean(dng * oh, axis=-1, keepdims=True))

    blk = pl.BlockSpec((t, HEAD_A), lambda j: (0, j))
    vec = pl.BlockSpec((1, HEAD_A), lambda j: (0, 0))
    return pl.pallas_call(
        body, name=name, grid=(N_HEADS_A,), in_specs=[blk, blk, vec, blk], out_specs=[blk, blk, vec],
        out_shape=[jax.ShapeDtypeStruct((t, WIDTH_A), F32), jax.ShapeDtypeStruct((t, WIDTH_A), BF16),
                   jax.ShapeDtypeStruct((1, HEAD_A), F32)],
        compiler_params=_params(("arbitrary",)))(o, gate, g, dy)


def _cmul(ar, ai, br, bi):
    return ar * br - ai * bi, ar * bi + ai * br


def _scan_tables(ar, ai, reverse):
    p1 = (ar, ai)
    p2 = _cmul(*p1, *p1)
    p4 = _cmul(*p2, *p2)
    p8 = _cmul(*p4, *p4)
    p3 = _cmul(*p2, *p1)
    p5 = _cmul(*p4, *p1)
    p6 = _cmul(*p4, *p2)
    p7 = _cmul(*p4, *p3)
    pows = [p1, p2, p3, p4, p5, p6, p7, p8]
    rows = lax.broadcasted_iota(jnp.int32, (8, ar.shape[1]), 0)
    tr = jnp.zeros((8, ar.shape[1]), F32)
    ti = jnp.zeros((8, ar.shape[1]), F32)
    for r in range(8):
        pw = pows[7 - r] if reverse else pows[r]
        tr = jnp.where(rows == r, pw[0], tr)
        ti = jnp.where(rows == r, pw[1], ti)
    return p1, p2, p4, p8, tr, ti


def _tile_scan(xr, xi, p1, p2, p4, reverse):
    rows = lax.broadcasted_iota(jnp.int32, xr.shape, 0)
    for s, (pr, pi) in ((1, p1), (2, p2), (4, p4)):
        if reverse:
            keep = rows < 8 - s
            sr, si = pltpu.roll(xr, 8 - s, 0), pltpu.roll(xi, 8 - s, 0)
        else:
            keep = rows >= s
            sr, si = pltpu.roll(xr, s, 0), pltpu.roll(xi, s, 0)
        sr, si = jnp.where(keep, sr, 0.0), jnp.where(keep, si, 0.0)
        mr, mi = _cmul(pr, pi, sr, si)
        xr, xi = xr + mr, xi + mi
    return xr, xi


def _s5_scan_fwd(bu, a, name, tb=512, comm=None):
    t = bu.shape[0]
    cb = SCAN_CB
    nt = t // tb

    def body(b_ref, a_ref, x_ref, carry):
        @pl.when(pl.program_id(1) == 0)
        def _():
            carry[...] = jnp.zeros_like(carry)

        ar, ai = a_ref[:, 0:cb], a_ref[:, cb:2 * cb]
        p1, p2, p4, p8, tr, ti = _scan_tables(ar, ai, False)

        def step(j, c):
            cr, ci = c
            i = pl.multiple_of(j * 8, 8)
            xr, xi = _tile_scan(b_ref[pl.ds(i, 8), 0:cb], b_ref[pl.ds(i, 8), cb:2 * cb], p1, p2, p4, False)
            mr, mi = _cmul(tr, ti, cr, ci)
            xr, xi = xr + mr, xi + mi
            x_ref[pl.ds(i, 8), 0:cb] = xr
            x_ref[pl.ds(i, 8), cb:2 * cb] = xi
            return xr[7:8, :], xi[7:8, :]

        cr, ci = lax.fori_loop(0, tb // 8, step, (carry[0:1, :], carry[1:2, :]), unroll=2)
        carry[0:1, :] = cr
        carry[1:2, :] = ci

    blk = pl.BlockSpec((tb, 2 * cb), lambda j, i: (i, j))
    return _call(
        body, name=name, grid=(SSM_CH // cb, nt),
        in_specs=[blk, pl.BlockSpec((1, 2 * cb), lambda j, i: (0, j))], out_specs=blk,
        out_shape=jax.ShapeDtypeStruct((t, 2 * SSM_CH), F32), scratch_shapes=[pltpu.VMEM((8, cb), F32)],
        sem=("parallel", "arbitrary"), args=(bu, a), comm=comm)


def _s5_scan_bwd(dx, x, a, name, tb=512, comm=None):
    t = dx.shape[0]
    cb = SCAN_CB
    nt = t // tb
    nj = tb // 8

    def body(d_ref, x_ref, xp_ref, a_ref, l_ref, da_ref, carry, acc):
        tblk = pl.program_id(1)

        @pl.when(tblk == 0)
        def _():
            carry[...] = jnp.zeros_like(carry)
            acc[...] = jnp.zeros_like(acc)

        ar, ai = a_ref[:, 0:cb], a_ref[:, cb:2 * cb]
        p1, p2, p4, p8, tr, ti = _scan_tables(ar, -ai, True)
        rows = lax.broadcasted_iota(jnp.int32, (8, cb), 0)

        def step(jj, c):
            cr, ci, sr_acc, si_acc = c
            j = nj - 1 - jj
            i = pl.multiple_of(j * 8, 8)
            lr, li = _tile_scan(d_ref[pl.ds(i, 8), 0:cb], d_ref[pl.ds(i, 8), cb:2 * cb], p1, p2, p4, True)
            mr, mi = _cmul(tr, ti, cr, ci)
            lr, li = lr + mr, li + mi
            l_ref[pl.ds(i, 8), 0:cb] = lr
            l_ref[pl.ds(i, 8), cb:2 * cb] = li
            ip = pl.multiple_of(jnp.maximum(j - 1, 0) * 8, 8)
            prev_r = jnp.where(j > 0, x_ref[pl.ds(ip, 8), 0:cb], xp_ref[:, 0:cb])
            prev_i = jnp.where(j > 0, x_ref[pl.ds(ip, 8), cb:2 * cb], xp_ref[:, cb:2 * cb])
            edge = jnp.where(jnp.logical_and(j == 0, tblk == nt - 1), 0.0, 1.0)
            xs_r = jnp.where(rows == 0, pltpu.roll(prev_r, 1, 0) * edge, pltpu.roll(x_ref[pl.ds(i, 8), 0:cb], 1, 0))
            xs_i = jnp.where(rows == 0, pltpu.roll(prev_i, 1, 0) * edge, pltpu.roll(x_ref[pl.ds(i, 8), cb:2 * cb], 1, 0))
            sr_acc = sr_acc + lr * xs_r + li * xs_i
            si_acc = si_acc + li * xs_r - lr * xs_i
            return lr[0:1, :], li[0:1, :], sr_acc, si_acc

        cr, ci, sr_acc, si_acc = lax.fori_loop(
            0, nj, step, (carry[0:1, :], carry[1:2, :], acc[:, 0:cb], acc[:, cb:2 * cb]))
        carry[0:1, :] = cr
        carry[1:2, :] = ci
        acc[:, 0:cb] = sr_acc
        acc[:, cb:2 * cb] = si_acc

        @pl.when(tblk == nt - 1)
        def _():
            da_ref[...] = jnp.sum(acc[...], axis=0, keepdims=True)

    blk = pl.BlockSpec((tb, 2 * cb), lambda j, i: (nt - 1 - i, j))
    prev = pl.BlockSpec((8, 2 * cb), lambda j, i: (jnp.maximum((nt - 1 - i) * (tb // 8) - 1, 0), j))
    vec = pl.BlockSpec((1, 2 * cb), lambda j, i: (0, j))
    return _call(
        body, name=name, grid=(SSM_CH // cb, nt), in_specs=[blk, blk, prev, vec], out_specs=[blk, vec],
        out_shape=[jax.ShapeDtypeStruct((t, 2 * SSM_CH), F32), jax.ShapeDtypeStruct((1, 2 * SSM_CH), F32)],
        scratch_shapes=[pltpu.VMEM((8, cb), F32), pltpu.VMEM((8, 2 * cb), F32)],
        sem=("parallel", "arbitrary"), args=(dx, x, x, a), comm=comm)


def _glu_fwd(yc, u, dvec, wg, bg, name, tr=256):
    t = yc.shape[0]

    def body(yc_ref, u_ref, d_ref, w_ref, b_ref, yl_ref, yb_ref):
        yl = yc_ref[...] + d_ref[...] * u_ref[...]
        yl_ref[...] = yl
        yg, _ = _gelu_and_grad(yl)
        z = jnp.dot(yg.astype(BF16), w_ref[...], preferred_element_type=F32) + b_ref[...]
        yb_ref[...] = (yg * _sigmoid(z)).astype(BF16)

    blk = pl.BlockSpec((tr, SSM_WIDTH), lambda i: (i, 0))
    vec = pl.BlockSpec((1, SSM_WIDTH), lambda i: (0, 0))
    return pl.pallas_call(
        body, name=name, grid=(t // tr,),
        in_specs=[blk, blk, vec, pl.BlockSpec((SSM_WIDTH, SSM_WIDTH), lambda i: (0, 0)), vec],
        out_specs=[blk, blk],
        out_shape=[jax.ShapeDtypeStruct((t, SSM_WIDTH), F32), jax.ShapeDtypeStruct((t, SSM_WIDTH), BF16)],
        compiler_params=_params(("parallel",)))(yc, u, dvec, wg, bg)


def _glu_bwd(yl, u, dvec, wg, bg, dyb, name, tr=256):
    t = yl.shape[0]

    def body(yl_ref, u_ref, d_ref, w_ref, b_ref, dy_ref, dyl_ref, du_ref, dw_ref, db_ref, dd_ref):
        @pl.when(pl.program_id(0) == 0)
        def _():
            dw_ref[...] = jnp.zeros_like(dw_ref)
            db_ref[...] = jnp.zeros_like(db_ref)
            dd_ref[...] = jnp.zeros_like(dd_ref)

        ylv, dyv, wv = yl_ref[...], dy_ref[...].astype(F32), w_ref[...]
        yg, dgelu = _gelu_and_grad(ylv)
        ygb = yg.astype(BF16)
        z = jnp.dot(ygb, wv, preferred_element_type=F32) + b_ref[...]
        sg = _sigmoid(z)
        dz = dyv * yg * sg * (1.0 - sg)
        dzb = dz.astype(BF16)
        dyg = dyv * sg + lax.dot_general(dzb, wv, (((1,), (1,)), ((), ())), preferred_element_type=F32)
        dyl = dyg * dgelu
        dyl_ref[...] = dyl.astype(BF16)
        du_ref[...] = dyl * d_ref[...]
        dw_ref[...] += lax.dot_general(ygb, dzb, (((0,), (0,)), ((), ())), preferred_element_type=F32)
        db_ref[...] += jnp.sum(dz, axis=0, keepdims=True)
        dd_ref[...] += jnp.sum(dyl * u_ref[...], axis=0, keepdims=True)

    blk = pl.BlockSpec((tr, SSM_WIDTH), lambda i: (i, 0))
    vec = pl.BlockSpec((1, SSM_WIDTH), lambda i: (0, 0))
    wsp = pl.BlockSpec((SSM_WIDTH, SSM_WIDTH), lambda i: (0, 0))
    return pl.pallas_call(
        body, name=name, grid=(t // tr,), in_specs=[blk, blk, vec, wsp, vec, blk],
        out_specs=[blk, blk, wsp, vec, vec],
        out_shape=[jax.ShapeDtypeStruct((t, SSM_WIDTH), BF16), jax.ShapeDtypeStruct((t, SSM_WIDTH), F32),
                   jax.ShapeDtypeStruct((SSM_WIDTH, SSM_WIDTH), F32), jax.ShapeDtypeStruct((1, SSM_WIDTH), F32),
                   jax.ShapeDtypeStruct((1, SSM_WIDTH), F32)],
        compiler_params=_params(("arbitrary",)))(yl, u, dvec, wg, bg, dyb)


def _mesh_pos():
    return lax.axis_index("x"), lax.axis_index("y"), lax.axis_index("c")


def _device_index():
    x, y, c = _mesh_pos()
    return 4 * x + 2 * y + c


def _gather_comm(arrays):
    na = len(arrays)

    def own_copy(ins, outs, sems, ai):
        return pltpu.make_async_copy(ins[ai], outs[ai].at[_device_index()], sems[2].at[ai])

    def ctx(ins, outs, sems):
        send_sems, recv_sems = sems[:2]
        x, y, c = _mesh_pos()
        chips = [(1 - x, y), (x, 1 - y), (1 - x, 1 - y)]

        def copy(ai, kk, block, to, own=False):
            slot = outs[ai].at[4 * block[0] + 2 * block[1] + block[2]]
            return pltpu.make_async_remote_copy(
                src_ref=ins[ai] if own else slot, dst_ref=slot, send_sem=send_sems.at[ai, kk],
                recv_sem=recv_sems.at[ai, kk], device_id=to, device_id_type=MESH)

        return (x, y, c), (x, y, 1 - c), chips, c, copy

    def start(ins, outs, sems):
        me, sibling, chips, c, copy = ctx(ins, outs, sems)
        for ai in range(na):
            copy(ai, 0, me, sibling, own=True).start()
            for j, chip in enumerate(chips):
                copy(ai, 1 + j, me, (*chip, c), own=True).start()
        for ai in range(na):
            own_copy(ins, outs, sems, ai).start()

    def mid(ins, outs, sems):
        me, sibling, chips, c, copy = ctx(ins, outs, sems)
        for ai in range(na):
            for j, chip in enumerate(chips):
                copy(ai, 1 + j, (*chip, c), me).wait_recv()
                copy(ai, 4 + j, (*chip, c), sibling).start()

    def end(ins, outs, sems):
        me, sibling, chips, c, copy = ctx(ins, outs, sems)
        for ai in range(na):
            copy(ai, 0, sibling, me).wait_recv()
            copy(ai, 0, me, sibling, own=True).wait_send()
            for j, chip in enumerate(chips):
                copy(ai, 4 + j, (*chip, 1 - c), me).wait_recv()
                copy(ai, 1 + j, me, (*chip, c), own=True).wait_send()
                copy(ai, 4 + j, (*chip, c), sibling).wait_send()
            own_copy(ins, outs, sems, ai).wait()

    return Comm(arrays, [jax.ShapeDtypeStruct((N_DEV,) + a.shape, a.dtype) for a in arrays],
                [pltpu.SemaphoreType.DMA((na, 7)), pltpu.SemaphoreType.DMA((na, 7)), pltpu.SemaphoreType.DMA((na,))],
                start, end, mid)


def _sequencer_gather(arrays, name, collective_id):
    comm = _gather_comm(arrays)
    na = len(arrays)

    def body(*refs):
        ins, outs, sems = refs[:na], refs[na:2 * na], refs[2 * na:]
        x, y, c = _mesh_pos()
        peers = [(x, y, 1 - c), (1 - x, y, c), (x, 1 - y, c), (1 - x, 1 - y, c)]
        barrier = pltpu.get_barrier_semaphore()
        for peer in peers:
            pl.semaphore_signal(barrier, inc=1, device_id=peer, device_id_type=MESH)
        pl.semaphore_wait(barrier, len(peers))
        comm.start(ins, outs, sems)
        comm.mid(ins, outs, sems)
        comm.end(ins, outs, sems)

    return list(pl.kernel(
        body, out_type=tuple(comm.out_shapes), mesh=plsc.ScalarSubcoreMesh(axis_name="sequencer", num_cores=1),
        name=name, scratch_types=tuple(comm.sems),
        compiler_params=pltpu.CompilerParams(collective_id=collective_id))(*arrays))


def _sequencer_exchange(comm, peers_of, name, collective_id):
    na = len(comm.inputs)

    def body(*refs):
        ins, outs, sems = refs[:na], refs[na:na + len(comm.out_shapes)], refs[na + len(comm.out_shapes):]
        peers = peers_of(*_mesh_pos())
        barrier = pltpu.get_barrier_semaphore()
        for peer in peers:
            pl.semaphore_signal(barrier, inc=1, device_id=peer, device_id_type=MESH)
        pl.semaphore_wait(barrier, len(peers))
        comm.start(ins, outs, sems)
        comm.end(ins, outs, sems)

    return list(pl.kernel(
        body, out_type=tuple(comm.out_shapes), mesh=plsc.ScalarSubcoreMesh(axis_name="sequencer", num_cores=1),
        name=name, scratch_types=tuple(comm.sems),
        compiler_params=pltpu.CompilerParams(collective_id=collective_id))(*comm.inputs))


SIBLING_SWAP_ID, CHIP_EXCHANGE_ID = 9, 10


def _sequencer_swap(arrays, name):
    return _sequencer_exchange(_swap_comm(arrays), lambda x, y, c: [(x, y, 1 - c)], name, SIBLING_SWAP_ID)[0]


def _sequencer_chips(send, name):
    return _sequencer_exchange(_chips_comm(send), lambda x, y, c: [(1 - x, y, c), (x, 1 - y, c), (1 - x, 1 - y, c)],
                               name, CHIP_EXCHANGE_ID)[0]


def _swap_comm(arrays):
    na = len(arrays)
    offs = np.concatenate([[0], np.cumsum([a.shape[1] for a in arrays])]).astype(int)

    def copies(ins, outs, sems):
        x, y, c = _mesh_pos()
        return [pltpu.make_async_remote_copy(
            src_ref=ins[ai].at[2 * k + 1 - c], dst_ref=outs[0].at[k, pl.ds(int(offs[ai]), arrays[ai].shape[1])],
            send_sem=sems[0].at[ai, k], recv_sem=sems[1].at[ai, k], device_id=(x, y, 1 - c), device_id_type=MESH)
            for ai in range(na) for k in range(4)]

    def start(ins, outs, sems):
        for cp in copies(ins, outs, sems):
            cp.start()

    def end(ins, outs, sems):
        for cp in copies(ins, outs, sems):
            cp.wait()

    return Comm(arrays, [jax.ShapeDtypeStruct((4, int(offs[-1]), PACK_COLS), arrays[0].dtype)],
                [pltpu.SemaphoreType.DMA((na, 4)), pltpu.SemaphoreType.DMA((na, 4))], start, end)


def _chips_comm(send):
    def copies(ins, outs, sems):
        x, y, c = _mesh_pos()
        chips = [(1 - x, y), (x, 1 - y), (1 - x, 1 - y)]
        return [pltpu.make_async_remote_copy(
            src_ref=ins[0].at[2 * cx + cy], dst_ref=outs[0].at[j], send_sem=sems[0].at[j], recv_sem=sems[1].at[j],
            device_id=(cx, cy, c), device_id_type=MESH) for j, (cx, cy) in enumerate(chips)]

    def start(ins, outs, sems):
        for cp in copies(ins, outs, sems):
            cp.start()

    def end(ins, outs, sems):
        for cp in copies(ins, outs, sems):
            cp.wait()

    return Comm([send], [jax.ShapeDtypeStruct((3,) + send.shape[1:], send.dtype)],
                [pltpu.SemaphoreType.DMA((3,)), pltpu.SemaphoreType.DMA((3,))], start, end)


def _pair_sum(keep, recv, name, tr=464):
    nchip, rows, cols = keep.shape

    def body(g_ref, r_ref, o_ref):
        o_ref[...] = (g_ref[...].astype(F32) + r_ref[...].astype(F32)).astype(BF16)

    blk = pl.BlockSpec((1, tr, cols), lambda k, i: (k, i, 0))
    return pl.pallas_call(
        body, name=name, grid=(nchip, rows // tr), in_specs=[blk, blk], out_specs=blk,
        out_shape=jax.ShapeDtypeStruct((nchip, rows, cols), BF16),
        compiler_params=_params(("parallel", "parallel")))(keep, recv)


def _pair_sum_pieces(pieces, recv, name, tr):
    _, rows, cols = pieces.shape
    core = lax.axis_index("c").astype(jnp.int32).reshape(1)

    def body(c_ref, g_ref, r_ref, o_ref):
        del c_ref
        o_ref[...] = (g_ref[...].astype(F32) + r_ref[...].astype(F32)).astype(BF16)

    grid_spec = pltpu.PrefetchScalarGridSpec(
        num_scalar_prefetch=1, grid=(4, rows // tr),
        in_specs=[pl.BlockSpec((1, tr, cols), lambda k, i, c_ref: (2 * k + c_ref[0], i, 0)),
                  pl.BlockSpec((1, tr, cols), lambda k, i, c_ref: (k, i, 0))],
        out_specs=pl.BlockSpec((1, tr, cols), lambda k, i, c_ref: (k, i, 0)))
    return pl.pallas_call(
        body, name=name, grid_spec=grid_spec, out_shape=jax.ShapeDtypeStruct((4, rows, cols), BF16),
        compiler_params=_params(("parallel", "parallel")))(core, pieces, recv)


def _chip_sum(own, others, name, tr=464):
    _, rows, cols = own.shape
    chip = (2 * lax.axis_index("x") + lax.axis_index("y")).astype(jnp.int32).reshape(1)

    def body(chip_ref, own_ref, oth_ref, o_ref):
        del chip_ref
        acc = own_ref[0].astype(F32)
        for j in range(3):
            acc = acc + oth_ref[j].astype(F32)
        o_ref[...] = acc

    grid_spec = pltpu.PrefetchScalarGridSpec(
        num_scalar_prefetch=1, grid=(rows // tr,),
        in_specs=[pl.BlockSpec((1, tr, cols), lambda i, chip_ref: (chip_ref[0], i, 0)),
                  pl.BlockSpec((3, tr, cols), lambda i, chip_ref: (0, i, 0))],
        out_specs=pl.BlockSpec((tr, cols), lambda i, chip_ref: (i, 0)))
    return pl.pallas_call(
        body, name=name, grid_spec=grid_spec, out_shape=jax.ShapeDtypeStruct((rows, cols), F32),
        compiler_params=_params(("parallel",)))(chip, own, others)


def _sum_leading(parts, name, tr=464):
    nparts, rows, cols = parts.shape
    tr = tr if rows % tr == 0 else rows

    def body(p_ref, o_ref):
        acc = p_ref[0].astype(F32)
        for i in range(1, nparts):
            acc = acc + p_ref[i].astype(F32)
        o_ref[...] = acc

    return pl.pallas_call(
        body, name=name, grid=(rows // tr,),
        in_specs=[pl.BlockSpec((nparts, tr, cols), lambda i: (0, i, 0))],
        out_specs=pl.BlockSpec((tr, cols), lambda i: (i, 0)), out_shape=jax.ShapeDtypeStruct((rows, cols), F32),
        compiler_params=_params(("parallel",)))(parts)


def _adamw(w, g, m, v, name, comm=None):
    shape = w.shape
    cols = shape[-1]
    lead = shape[0] if len(shape) >= 3 else 1
    rows = int(np.prod(shape[:-1])) // lead if len(shape) > 1 else 1
    w2, g2, m2, v2 = (a.reshape(lead, rows, cols) for a in (w, g, m, v))
    tr = max([t for t in range(8, min(rows, 512) + 1, 8) if rows % t == 0], default=rows)
    bc1, bc2 = 1.0 - ADAM_B1 ** ADAM_STEP, 1.0 - ADAM_B2 ** ADAM_STEP

    def body(w_ref, g_ref, m_ref, v_ref, d_ref, nm_ref, nv_ref):
        gv = g_ref[...]
        nm = ADAM_B1 * m_ref[...] + (1.0 - ADAM_B1) * gv
        nv = ADAM_B2 * v_ref[...] + (1.0 - ADAM_B2) * (gv * gv)
        nm_ref[...] = nm
        nv_ref[...] = nv
        d_ref[...] = -ADAM_LR * ((nm / bc1) / (jnp.sqrt(nv / bc2) + ADAM_EPS) + ADAM_WD * w_ref[...])

    blk = pl.BlockSpec((1, tr, cols), lambda l, i: (l, i, 0))
    res = _call(body, name=name, grid=(lead, rows // tr), in_specs=[blk] * 4, out_specs=[blk] * 3,
                out_shape=[jax.ShapeDtypeStruct((lead, rows, cols), F32)] * 3, sem=("parallel", "parallel"),
                args=(w2, g2, m2, v2), comm=comm)
    outs, couts = res if comm is not None else (res, None)
    outs = tuple(o.reshape(shape) for o in outs)
    return outs if comm is None else (outs, couts)


WEIGHT_NAMES = ['norm_mix_g', 'norm_xa_g', 'norm_ffn_g', 'norm_mem_g', 'norm_final_g', 'w_in_ab', 'conv_qkv_a',
                'a_log_a', 'dt_bias_a', 'onorm_g_a', 'ssm_lambda_re', 'ssm_lambda_im', 'ssm_b_re', 'ssm_b_im',
                'ssm_c_re', 'ssm_c_im', 'ssm_d', 'ssm_log_dt', 'w_glu_b', 'b_glu_b', 'w_out_ab', 'pool_w',
                'pool_scale', 'xa_wq', 'xa_wkv', 'xa_wo', 'ffn_w_up', 'ffn_conv', 'ffn_w_down']
BIG_SHARDED = {'w_in_ab': ((1, 1024, 2568), 2), 'w_glu_b': ((1, 512, 512), 1), 'w_out_ab': ((1, 1024, 1024), 1),
               'pool_w': ((1, 4, 256, 256), 2), 'xa_wq': ((2, 1024, 1024), 1), 'xa_wkv': ((2, 1024, 2048), 2),
               'xa_wo': ((2, 1024, 1024), 1), 'ffn_w_up': ((2, 1024, 5632), 2), 'ffn_w_down': ((2, 2816, 1024), 1)}
SMALL_SHARDED = {'conv_qkv_a': ((1, 4, 1536), 2), 'pool_scale': ((1, 1024), 1), 'ffn_conv': ((2, 3, 5632), 2)}
REPLICATED = {'norm_mix_g': (2, 1024), 'norm_xa_g': (2, 1024), 'norm_ffn_g': (2, 1024), 'norm_mem_g': (1024,),
              'norm_final_g': (1024,), 'a_log_a': (1, 4), 'dt_bias_a': (1, 4), 'onorm_g_a': (1, 128),
              'ssm_lambda_re': (1, 32, 64), 'ssm_lambda_im': (1, 32, 64), 'ssm_b_re': (1, 32, 64, 16),
              'ssm_b_im': (1, 32, 64, 16), 'ssm_c_re': (1, 32, 16, 64), 'ssm_c_im': (1, 32, 16, 64),
              'ssm_d': (1, 32, 16), 'ssm_log_dt': (1, 32), 'b_glu_b': (1, 512)}
PACK_ROW_ALIGN = 8


def _shard_shape(shape, axis):
    return tuple(s // N_DEV if i == axis else s for i, s in enumerate(shape))


def _round_up(n, m):
    return (n + m - 1) // m * m


def _pack(arrays):
    total = sum(int(np.prod(a.shape)) for a in arrays)
    padded = _round_up(total, PACK_COLS * PACK_ROW_ALIGN)
    parts = [a.astype(F32).reshape(-1) for a in arrays]
    if padded != total:
        parts.append(jnp.zeros((padded - total,), F32))
    return jnp.concatenate(parts).reshape(padded // PACK_COLS, PACK_COLS)


def _unpack(packed, shapes):
    flat, out, off = packed.reshape(-1), [], 0
    for shape in shapes:
        size = int(np.prod(shape))
        out.append(flat[off:off + size].reshape(shape))
        off += size
    return out


def _split_shards(full, axis):
    shape = full.shape
    s = shape[axis] // N_DEV
    a = full.reshape(shape[:axis] + (N_DEV, s) + shape[axis + 1:])
    return jnp.moveaxis(a, axis, 0).reshape(N_DEV, -1)


def _merge_shards(pieces, shape, axis):
    sh = _shard_shape(shape, axis)
    a = pieces.reshape((N_DEV,) + sh)
    a = jnp.moveaxis(a, 0, axis)
    return a.reshape(shape)


_SCAN_NB = SSM_CH // SCAN_CB


def _to_scan_layout(m, axis):
    shape = m.shape
    m = m.reshape(shape[:axis] + (2, _SCAN_NB, SCAN_CB) + shape[axis + 1:])
    return jnp.swapaxes(m, axis, axis + 1).reshape(shape)


def _from_scan_layout(m, axis):
    shape = m.shape
    m = m.reshape(shape[:axis] + (_SCAN_NB, 2, SCAN_CB) + shape[axis + 1:])
    return jnp.swapaxes(m, axis, axis + 1).reshape(shape)


def _s5_discretise(lam_re, lam_im, b_re, b_im, log_dt):
    dt = jnp.exp(log_dt)[:, None]
    mag = jnp.exp(lam_re * dt)
    ang = lam_im * dt
    lb_re, lb_im = mag * jnp.cos(ang), mag * jnp.sin(ang)
    den = lam_re * lam_re + lam_im * lam_im
    nr, ni = lb_re - 1.0, lb_im
    coef_re = (nr * lam_re + ni * lam_im) / den
    coef_im = (ni * lam_re - nr * lam_im) / den
    bb_re = coef_re[..., None] * b_re - coef_im[..., None] * b_im
    bb_im = coef_re[..., None] * b_im + coef_im[..., None] * b_re
    return lb_re, lb_im, bb_re, bb_im


_GROUPS_PER_BLOCK = N_GROUPS // _SCAN_NB
_U_BLOCK = _GROUPS_PER_BLOCK * SSM_GROUP


def _s5_matrices(lb_re, lb_im, bb_re, bb_im, c_re, c_im):
    eye = jnp.eye(_GROUPS_PER_BLOCK, dtype=F32)
    blocked = lambda m: m.reshape((_SCAN_NB, _GROUPS_PER_BLOCK) + m.shape[1:])
    bmat = lambda bb: jnp.einsum('jgph,gk->jghkp', blocked(bb), eye).reshape(_SCAN_NB, _U_BLOCK, SCAN_CB)
    cmat = lambda cc: jnp.einsum('jghp,gk->jkpgh', blocked(cc), eye).reshape(_SCAN_NB, SCAN_CB, _U_BLOCK)
    b_in = jnp.concatenate([bmat(bb_re), bmat(bb_im)], axis=2)
    c_out = jnp.concatenate([cmat(c_re), -cmat(c_im)], axis=1)
    a_row = _to_scan_layout(jnp.concatenate([lb_re.reshape(1, SSM_CH), lb_im.reshape(1, SSM_CH)], axis=1), 1)
    return b_in, c_out, a_row


def _s5_matrix_grads(db_in, dc_out, da_row):
    da_nat = _from_scan_layout(da_row, 1)
    eye = jnp.eye(_GROUPS_PER_BLOCK, dtype=F32)
    nb, gb = _SCAN_NB, _GROUPS_PER_BLOCK
    bgrad = lambda m: jnp.einsum('jghkp,gk->jgph', m.reshape(nb, gb, SSM_GROUP, gb, SSM_STATE), eye
                                 ).reshape(N_GROUPS, SSM_STATE, SSM_GROUP)
    cgrad = lambda m: jnp.einsum('jkpgh,gk->jghp', m.reshape(nb, gb, SSM_STATE, gb, SSM_GROUP), eye
                                 ).reshape(N_GROUPS, SSM_GROUP, SSM_STATE)
    dbb_re, dbb_im = bgrad(db_in[:, :, :SCAN_CB]), bgrad(db_in[:, :, SCAN_CB:])
    dc_re, dc_im = cgrad(dc_out[:, :SCAN_CB]), -cgrad(dc_out[:, SCAN_CB:])
    dlb_re = da_nat[0, :SSM_CH].reshape(N_GROUPS, SSM_STATE)
    dlb_im = da_nat[0, SSM_CH:].reshape(N_GROUPS, SSM_STATE)
    return dlb_re, dlb_im, dbb_re, dbb_im, dc_re, dc_im


def _as_pieces(a):
    return a.reshape(N_DEV, a.shape[0] // N_DEV, a.shape[1])


def _hybrid_fwd(xn, x, wts, p, weights, riders):
    sv = {}
    hq = _mm(xn, wts['w_qkv_t'], "nt", "l0_in_qkv")
    gate = _mm(xn, wts['w_gate_t'], "nt", "l0_in_gate")
    ba = _mm(xn, wts['w_ba_t'], "nt", "l0_in_ba")
    u = _mm(xn, wts['w_u_t'], "nt", "l0_in_u")
    conv = p['conv_qkv']
    qkv = _qkv_pre_fwd(hq, conv, "l0_qkv_pre")
    gates = _gates_fwd(ba, p['arow'], p['brow'], "l0_gates")
    o, tm_all, s_all = riders.run("l0_gdr_fwd", _gdr_fwd, qkv, gates)
    wts['w_glu'], wts['w_out'] = weights.full['w_glu'], weights.full['w_out']
    y_a = _onorm_fwd(o, gate, p['onorm_g'], "l0_onorm")
    bu = riders.run("l0_s5_bu", _mm_bd, u, p['b_in'], "nn")
    xs = riders.run("l0_s5_scan", _s5_scan_fwd, bu, p['a_row'])
    weights.gather_by_sequencer(GATHER_LAYER1, xs, "gather_layer1", GATHER_LAYER1_ID)
    yc = riders.run("l0_s5_cx", _mm_bd, xs, p['c_out'], "nn")
    yl, y_b = _glu_fwd(yc, u, p['d_row'], wts['w_glu'], p['b_glu'], "l0_glu")
    mixed = jnp.concatenate([y_a, y_b], axis=1)
    x1 = _mm(mixed, wts['w_out'], "nn", "l0_out", res=x)
    sv.update(hq=hq, gate=gate, ba=ba, u=u, qkv=qkv, gb=gates, o=o, tm=tm_all, s=s_all, xs=xs, yl=yl, mixed=mixed)
    return x1, sv


def _hybrid_bwd(dx1, xn, wts, p, sv, riders):
    gr = {}
    dmixed = _mm(dx1, wts['w_out'], "nt", "l0_out_dx", out_dtype=BF16)
    riders.grad('w_out', _as_pieces(_mm(sv['mixed'], dx1, "tn", "l0_out_dw", out_dtype=BF16)))
    dya, dyb = dmixed[:, :WIDTH_A], dmixed[:, WIDTH_A:]
    dyl, du_direct, dw_glu, gr['b_glu_b'], dd = _glu_bwd(
        sv['yl'], sv['u'], p['d_row'], wts['w_glu'], p['b_glu'], dyb, "l0_glu_bwd")
    riders.grad('w_glu', dw_glu.astype(BF16).reshape(N_DEV, -1, PACK_COLS))
    dxs = riders.run("l0_s5_cx_dx", _mm_bd, dyl, p['c_out'], "nt")
    dc_out = _mm_bd(sv['xs'], dyl, "tn", "l0_s5_cx_dw")
    lam, da_row = riders.run("l0_s5_scan_bwd", _s5_scan_bwd, dxs, sv['xs'], p['a_row'])
    du = _mm_bd(lam, p['b_in'], "nt", "l0_s5_bu_dx", res=du_direct, out_dtype=BF16)
    db_in = _mm_bd(sv['u'], lam, "tn", "l0_s5_bu_dw")
    gr['s5'] = (db_in, dc_out, da_row, dd)
    do, dgate, gr['onorm_g_a'] = _onorm_bwd(sv['o'], sv['gate'], p['onorm_g'], dya, "l0_onorm_bwd")
    dqkv, dgb = riders.run("l0_gdr_bwd", _gdr_bwd, sv['qkv'], sv['gb'], sv['tm'], sv['s'], do)
    dhq, gr['conv_qkv_a'] = _qkv_pre_bwd(sv['hq'], p['conv_qkv'], dqkv, "l0_qkv_pre_bwd")
    dba, da_log, ddt_bias = _gates_bwd(sv['ba'], p['arow'], p['brow'], dgb, "l0_gates_bwd")
    gr['a_log_a'], gr['dt_bias_a'] = da_log[:, 4:8], ddt_bias[:, 4:8]
    dw_qkv_t = _mm(dhq, xn, "tn", "l0_in_qkv_dw", out_dtype=BF16)
    dw_gate_t = _mm(dgate, xn, "tn", "l0_in_gate_dw", out_dtype=BF16)
    dw_ba_t = _mm(dba, xn, "tn", "l0_in_ba_dw", out_dtype=BF16)
    dw_u_t = _mm(du, xn, "tn", "l0_in_u_dw", out_dtype=BF16)
    dw_in_t = _as_pieces(jnp.concatenate([dw_qkv_t, dw_gate_t, dw_ba_t[:8], dw_u_t], axis=0))
    riders.grad('w_in_t', jnp.concatenate(
        [dw_in_t, jnp.zeros((N_DEV, dict(PIECES)['w_in_t'] - W_IN_PIECE, D_MODEL), BF16)], axis=1))
    dxn = riders.run("l0_in_qkv_dx", _mm, dhq, wts['w_qkv_t'], "nn")
    dxn = _mm(dgate, wts['w_gate_t'], "nn", "l0_in_gate_dx", res=dxn)
    dxn = _mm(dba, wts['w_ba_t'], "nn", "l0_in_ba_dx", res=dxn)
    dxn = riders.run("l0_in_u_dx", _mm, du, wts['w_u_t'], "nn", res=dxn)
    return dxn, gr


def _xa_fwd(x1, g, mem_n, wq, wkv_t, wo, tag, riders):
    xq = _rms_fwd(x1, g, BF16, tag + "_norm")
    q = _mm(xq, wq, "nn", tag + "_q", out_dtype=BF16)
    kv = _mm(mem_n, wkv_t, "nt", tag + "_kv", out_dtype=BF16)
    o = riders.run(tag + "_attn", _attn_fwd, q, kv)
    x2 = _mm(o, wo, "nn", tag + "_o", res=x1)
    return x2, dict(xq=xq, q=q, kv=kv, o=o)


def _xa_bwd(dx2, x1, g, mem_n, wq, wkv_t, wo, sv, tag, layer, riders):
    do = _mm(dx2, wo, "nt", tag + "_o_dx", out_dtype=BF16)
    riders.grad('wo%d' % layer, _as_pieces(_mm(sv['o'], dx2, "tn", tag + "_o_dw", out_dtype=BF16)))
    dq, dk, dv = _attn_bwd(sv['q'], sv['kv'], do, tag + "_attn_bwd")
    dkv = jnp.concatenate([dk, dv], axis=1).astype(BF16)
    dxq = _mm(dq, wq, "nt", tag + "_q_dx")
    riders.grad('wq%d' % layer, _as_pieces(_mm(sv['xq'], dq, "tn", tag + "_q_dw", out_dtype=BF16)))
    dmem_n = _mm(dkv, wkv_t, "nn", tag + "_kv_dx")
    riders.grad('wkv_t%d' % layer, _as_pieces(_mm(dkv, mem_n, "tn", tag + "_kv_dw", out_dtype=BF16)))
    dx1, dg = riders.run(tag + "_norm_bwd", _rms_bwd, x1, g, dxq, dx2)
    return dx1, dmem_n, dg


def _ffn_fwd(x2, g, w_up_t, conv, w_down, tag, riders):
    xf = _rms_fwd(x2, g, BF16, tag + "_norm")
    h = riders.run(tag + "_up", _mm, xf, w_up_t, "nt")
    a = riders.run(tag + "_act", _ffn_act_fwd, h, conv)
    x3 = _mm(a, w_down, "nn", tag + "_down", res=x2)
    return x3, dict(xf=xf, h=h, a=a)


def _ffn_bwd(dx3, x2, g, w_up_t, conv, w_down, sv, tag, layer, riders):
    da = _mm(dx3, w_down, "nt", tag + "_down_dx")
    riders.grad('down%d' % layer, _as_pieces(_mm(sv['a'], dx3, "tn", tag + "_down_dw", out_dtype=BF16)))
    dh, dconv = riders.run(tag + "_act_bwd", _ffn_act_bwd, sv['h'], conv, da)
    dxf = riders.run(tag + "_up_dx", _mm_parts, dh, w_up_t, "nn")
    dw_up_t = riders.run(tag + "_up_dw", _mm_parts, dh, sv['xf'], "tn", out_dtype=BF16)
    riders.grad('up_t%d' % layer, _as_pieces(dw_up_t))
    dx2, dg = riders.run(tag + "_norm_bwd", _rms_bwd, x2, g, dxf, dx3)
    return dx2, dconv, dg


BIG_NAMES, SMALL_NAMES, REP_NAMES = list(BIG_SHARDED), list(SMALL_SHARDED), list(REPLICATED)
SMALL_SIZES = [int(np.prod(_shard_shape(*SMALL_SHARDED[n]))) for n in SMALL_NAMES]


PIECES = [('w_in_t', 384), ('w_glu', 32), ('w_out', 128), ('pool_w', 32), ('wq0', 128), ('wq1', 128),
          ('wkv_t0', 256), ('wkv_t1', 256), ('wo0', 128), ('wo1', 128), ('up_t0', 704), ('up_t1', 704),
          ('down0', 352), ('down1', 352)]
W_IN_ROWS = 4 * WIDTH_A + 2 * N_HEADS_A + SSM_WIDTH
W_IN_PIECE = W_IN_ROWS // N_DEV


def _row_tile(rows):
    return max(t for t in range(16, min(rows, 512) + 1, 16) if rows % t == 0)


class _Riders:
    def __init__(self):
        self.waiting = {}
        self.deferred = {}
        self.grads = {}
        self.groups = []
        self.reduced = {}

    def add(self, host, comm, then):
        self.waiting.setdefault(host, []).append((comm, then))

    def after(self, marker, then):
        self.deferred.setdefault(marker, []).append(then)

    def mark(self, name, out=None):
        for cont in self.deferred.pop(name, []):
            step = cont()
            if step is not None:
                values, then = step
                out, values = lax.optimization_barrier((out, values))
                then(values)
        return out

    def run(self, name, fn, *args, **kw):
        riders = self.waiting.pop(name, [])
        if not riders:
            out = fn(*args, name=name, **kw)
        else:
            out, couts = fn(*args, name=name, comm=[c for c, _ in riders], **kw)
            for (_, then), got in zip(riders, couts):
                then(got)
        return self.mark(name, out)

    def grad(self, key, pieces):
        self.grads[key] = pieces
        for group in [g for g in self.groups if all(k in self.grads for k in g[1])]:
            self.groups.remove(group)
            self._reduce(*group)

    def _reduce(self, name, keys, pair_marker, sum_marker):
        arrays = [self.grads[k] for k in keys]
        rows = sum(a.shape[1] for a in arrays)
        tile = _row_tile(rows)
        from_sibling = _sequencer_swap(arrays, name + "_to_sibling")

        def after_swap():
            if len(arrays) == 1:
                chip_sums = _pair_sum_pieces(arrays[0], from_sibling, name + "_pair_sum", tr=tile)
            else:
                core = lax.axis_index("c")
                keep = jnp.concatenate(
                    [lax.dynamic_index_in_dim(a.reshape(4, 2, a.shape[1], PACK_COLS), core, 1, keepdims=False)
                     for a in arrays], axis=1)
                chip_sums = _pair_sum(keep, from_sibling, name + "_pair_sum", tr=tile)

            def exchange_among_chips(chip_sums):
                from_chips = _sequencer_chips(chip_sums, name + "_to_chips")

                def store(total):
                    off = 0
                    for k, a in zip(keys, arrays):
                        self.reduced[k] = total[off:off + a.shape[1]]
                        off += a.shape[1]

                self.after(sum_marker, lambda: (
                    _chip_sum(chip_sums, from_chips, name + "_chip_sum", tr=tile), store))

            return chip_sums, exchange_among_chips

        self.after(pair_marker, after_swap)


class _Weights:
    def __init__(self, inp):
        bf = lambda a: a.astype(BF16)
        local = {'w_in_t': bf(inp['w_in_ab'][0]).T, 'w_glu': bf(inp['w_glu_b'][0]), 'w_out': bf(inp['w_out_ab'][0]),
                 'pool_w': bf(inp['pool_w'][0]),
                 'small': _pack([inp[n] for n in SMALL_NAMES])}
        for l in range(2):
            local['wq%d' % l] = bf(inp['xa_wq'][l])
            local['wkv_t%d' % l] = bf(inp['xa_wkv'][l]).T
            local['wo%d' % l] = bf(inp['xa_wo'][l])
            local['up_t%d' % l] = bf(inp['ffn_w_up'][l]).T
            local['down%d' % l] = bf(inp['ffn_w_down'][l])
        self.local, self.full = local, {}

    def plan(self, keys):
        return _gather_comm([self.local[k] for k in keys])

    def gather_by_sequencer(self, keys, after, name, collective_id):
        arrays = [self.local[k] for k in keys]
        tie = (after.reshape(-1)[0] * 0.0).astype(arrays[0].dtype)
        arrays[0] = arrays[0] + tie
        self.land(keys, _sequencer_gather(arrays, name, collective_id))

    def land(self, keys, gathered):
        for k, g in zip(keys, gathered):
            if k == 'small':
                off = 0
                for n, size in zip(SMALL_NAMES, SMALL_SIZES):
                    self.full[n] = _merge_shards(g.reshape(N_DEV, -1)[:, off:off + size], *SMALL_SHARDED[n])
                    off += size
            elif k == 'pool_w':
                self.full[k] = jnp.swapaxes(g, 0, 1).reshape(len(POOL_WINDOWS), POOL_GROUP, POOL_GROUP)
            else:
                self.full[k] = g.reshape(N_DEV * g.shape[1], g.shape[2])


GATHER_FIRST = ['w_in_t', 'small']
GATHER_LAYER0 = ['w_glu', 'w_out', 'wq0', 'wkv_t0', 'wo0', 'down0', 'up_t0']
GATHER_LAYER1 = ['pool_w', 'wq1', 'wkv_t1', 'wo1', 'up_t1', 'down1']
GATHER_LAYER0_ID, GATHER_LAYER1_ID = 7, 8
GRAD_RIDES = [('g_down1', ['down1'], 'l1_ffn_up_dx', 'l1_xa_norm_bwd'),
              ('g_up1', ['up_t1'], 'l1_xa_norm_bwd', 'l0_ffn_up_dx'),
              ('g_xa1', ['wq1', 'wkv_t1', 'wo1', 'pool_w'], 'l0_ffn_up_dx', 'l0_xa_norm_bwd'),
              ('g_down0', ['down0'], 'l0_ffn_up_dx', 'l0_s5_scan_bwd'),
              ('g_l0', ['up_t0', 'wq0', 'wkv_t0', 'wo0'], 'l0_s5_cx_dx', 'l0_in_u_dx'),
              ('g_out', ['w_out', 'w_glu'], 'l0_s5_scan_bwd', 'l0_in_u_dx'),
              ('g_in', ['w_in_t'], 'l0_in_u_dx', 'adamw_pool_w')]


def _local_step(inp):
    f32_of = lambda n: inp[n].astype(F32)
    weights = _Weights(inp)
    riders = _Riders()
    riders.groups = list(GRAD_RIDES)
    full = weights.full
    weights.land(GATHER_FIRST, _comm_only(weights.plan(GATHER_FIRST), "gather_first"))
    weights.gather_by_sequencer(GATHER_LAYER0, full['w_in_t'], "gather_layer0", GATHER_LAYER0_ID)
    w_in_t = full['w_in_t']
    wts0 = dict(w_qkv_t=w_in_t[:3 * WIDTH_A], w_gate_t=w_in_t[3 * WIDTH_A:4 * WIDTH_A],
                w_ba_t=jnp.concatenate([w_in_t[4 * WIDTH_A:4 * WIDTH_A + 8], jnp.zeros((LANE - 8, D_MODEL), BF16)], 0),
                w_u_t=w_in_t[4 * WIDTH_A + 8:])
    lb_disc, disc_vjp = jax.vjp(_s5_discretise, f32_of('ssm_lambda_re')[0], f32_of('ssm_lambda_im')[0],
                                f32_of('ssm_b_re')[0], f32_of('ssm_b_im')[0], f32_of('ssm_log_dt')[0])
    b_in, c_out, a_row = _s5_matrices(*lb_disc, f32_of('ssm_c_re')[0], f32_of('ssm_c_im')[0])
    zeros4 = jnp.zeros((1, 4), F32)
    p0 = dict(conv_qkv=full['conv_qkv_a'][0], onorm_g=f32_of('onorm_g_a'),
              arow=jnp.concatenate([zeros4, f32_of('a_log_a'), jnp.zeros((1, LANE - 8), F32)], 1),
              brow=jnp.concatenate([zeros4, f32_of('dt_bias_a'), jnp.zeros((1, LANE - 8), F32)], 1),
              b_in=b_in.astype(BF16), c_out=c_out.astype(BF16), a_row=a_row,
              d_row=f32_of('ssm_d').reshape(1, SSM_WIDTH), b_glu=f32_of('b_glu_b'))

    x0 = inp['x'][0]
    mem_n = _rms_fwd(inp['mem'][0], inp['norm_mem_g'], BF16, "mem_norm")
    xn0 = _rms_fwd(x0, inp['norm_mix_g'][0], BF16, "l0_mix_norm")
    x1, sv_mix0 = _hybrid_fwd(xn0, x0, wts0, p0, weights, riders)
    x2, sv_xa0 = _xa_fwd(x1, inp['norm_xa_g'][0], mem_n, full['wq0'], full['wkv_t0'], full['wo0'], "l0_xa", riders)
    x3, sv_ffn0 = _ffn_fwd(x2, inp['norm_ffn_g'][0], full['up_t0'], full['ffn_conv'][0], full['down0'], "l0_ffn", riders)
    xn1 = _rms_fwd(x3, inp['norm_mix_g'][1], F32, "l1_mix_norm")
    x4 = _pool_fwd(xn1, full['pool_w'], full['pool_scale'], x3, "l1_pool")
    x5, sv_xa1 = _xa_fwd(x4, inp['norm_xa_g'][1], mem_n, full['wq1'], full['wkv_t1'], full['wo1'], "l1_xa", riders)
    x6, sv_ffn1 = _ffn_fwd(x5, inp['norm_ffn_g'][1], full['up_t1'], full['ffn_conv'][1], full['down1'], "l1_ffn", riders)
    loss_part, dx6, dg_final = _loss_head(x6, inp['norm_final_g'], inp['loss_target'][0], "loss_head")

    dx5, dconv1, dg_ffn1 = _ffn_bwd(dx6, x5, inp['norm_ffn_g'][1], full['up_t1'], full['ffn_conv'][1], full['down1'],
                                    sv_ffn1, "l1_ffn", 1, riders)
    dx4, dmem1, dg_xa1 = _xa_bwd(dx5, x4, inp['norm_xa_g'][1], mem_n, full['wq1'], full['wkv_t1'], full['wo1'],
                                 sv_xa1, "l1_xa", 1, riders)
    dxn1, dpool_w, dpool_scale = _pool_bwd(xn1, full['pool_w'], full['pool_scale'], dx4, "l1_pool_bwd")
    pool_pieces = jnp.swapaxes(dpool_w.astype(BF16).reshape(len(POOL_WINDOWS), N_DEV, -1, POOL_GROUP), 0, 1)
    riders.grad('pool_w', pool_pieces.reshape(N_DEV, -1, PACK_COLS))
    dx3, dg_mix1 = riders.run("l1_mix_norm_bwd", _rms_bwd, x3, inp['norm_mix_g'][1], dxn1, dx4)
    dx2, dconv0, dg_ffn0 = _ffn_bwd(dx3, x2, inp['norm_ffn_g'][0], full['up_t0'], full['ffn_conv'][0], full['down0'],
                                    sv_ffn0, "l0_ffn", 0, riders)
    dx1, dmem0, dg_xa0 = _xa_bwd(dx2, x1, inp['norm_xa_g'][0], mem_n, full['wq0'], full['wkv_t0'], full['wo0'],
                                 sv_xa0, "l0_xa", 0, riders)
    dxn0, g_mix0 = _hybrid_bwd(dx1, xn0, wts0, p0, sv_mix0, riders)
    grad_x, dg_mix0 = _rms_bwd(x0, inp['norm_mix_g'][0], dxn0, dx1, "l0_mix_norm_bwd")
    _, dg_mem = _rms_bwd(inp['mem'][0], inp['norm_mem_g'], dmem0 + dmem1, None, "mem_norm_bwd")
    assert not riders.groups and not riders.waiting and all(k.startswith("adamw_") for k in riders.deferred), (
        riders.groups, list(riders.waiting), list(riders.deferred))

    db_in, dc_out, da_row, dd = g_mix0['s5']
    dlb_re, dlb_im, dbb_re, dbb_im, dc_re, dc_im = _s5_matrix_grads(db_in, dc_out, da_row)
    dlam_re, dlam_im, dbr, dbi, dlog_dt = disc_vjp((dlb_re, dlb_im, dbb_re, dbb_im))

    rep_grads = {
        'norm_mix_g': jnp.concatenate([dg_mix0, dg_mix1], 0), 'norm_xa_g': jnp.concatenate([dg_xa0, dg_xa1], 0),
        'norm_ffn_g': jnp.concatenate([dg_ffn0, dg_ffn1], 0), 'norm_mem_g': dg_mem.reshape(-1),
        'norm_final_g': dg_final.reshape(-1), 'a_log_a': g_mix0['a_log_a'], 'dt_bias_a': g_mix0['dt_bias_a'],
        'onorm_g_a': g_mix0['onorm_g_a'], 'ssm_lambda_re': dlam_re[None], 'ssm_lambda_im': dlam_im[None],
        'ssm_b_re': dbr[None], 'ssm_b_im': dbi[None], 'ssm_c_re': dc_re[None], 'ssm_c_im': dc_im[None],
        'ssm_d': dd.reshape(1, N_GROUPS, SSM_GROUP), 'ssm_log_dt': dlog_dt[None], 'b_glu_b': g_mix0['b_glu_b']}
    small_grads = {'conv_qkv_a': g_mix0['conv_qkv_a'][None], 'pool_scale': dpool_scale,
                   'ffn_conv': jnp.stack([dconv0, dconv1])}
    return loss_part, grad_x, riders, rep_grads, small_grads


ADAMW_ORDER = ['ffn_w_up', 'ffn_w_down', 'xa_wkv', 'xa_wq', 'xa_wo', 'w_out_ab', 'w_glu_b', 'pool_w', 'w_in_ab']


def _update(inp, loss_part, grad_x, riders, rep_grads, small_grads):
    dev = _device_index()
    misc_local = _pack([rep_grads[n] for n in REP_NAMES] + [small_grads[n] for n in SMALL_NAMES] + [loss_part])
    (misc_all,) = _sequencer_gather([misc_local], "gather_small_grads", GATHER_LAYER0_ID)
    piece = lambda key: riders.reduced[key]
    both = lambda name: jnp.stack([piece(name + '0'), piece(name + '1')])
    swap = lambda a: jnp.swapaxes(a, -1, -2)
    reduced = {'w_in_ab': lambda: piece('w_in_t')[:W_IN_PIECE][None],
               'w_glu_b': lambda: piece('w_glu').reshape(inp['w_glu_b'].shape),
               'w_out_ab': lambda: piece('w_out')[None], 'pool_w': lambda: piece('pool_w').reshape(inp['pool_w'].shape),
               'xa_wq': lambda: both('wq'), 'xa_wkv': lambda: both('wkv_t'), 'xa_wo': lambda: both('wo'),
               'ffn_w_up': lambda: both('up_t'), 'ffn_w_down': lambda: both('down')}
    transposed = ('w_in_ab', 'xa_wkv', 'ffn_w_up')
    grads, upd = {}, {}
    assert sorted(ADAMW_ORDER) == sorted(BIG_NAMES)
    for n in ADAMW_ORDER:
        fix = swap if n in transposed else (lambda a: a)
        g = reduced[n]()
        out = riders.run("adamw_" + n, _adamw, fix(inp[n]), g, fix(inp['m_' + n]), fix(inp['v_' + n]))
        upd[n], grads[n] = tuple(fix(o) for o in out), fix(g)
    assert not riders.waiting and not riders.deferred, (list(riders.waiting), list(riders.deferred))
    misc_sum = _sum_leading(misc_all, "small_grads_sum")
    misc = _unpack(misc_sum, [inp[n].shape for n in REP_NAMES] + [SMALL_SHARDED[n][0] for n in SMALL_NAMES] + [()])
    loss = misc.pop()
    for n, g in zip(REP_NAMES, misc):
        grads[n] = g
    for n, g in zip(SMALL_NAMES, misc[len(REP_NAMES):]):
        grads[n] = lax.dynamic_index_in_dim(_split_shards(g, SMALL_SHARDED[n][1]), dev, 0, keepdims=False
                                            ).reshape(inp[n].shape)
    tiny_names = REP_NAMES + SMALL_NAMES
    rep_total = sum(int(np.prod(inp[n].shape)) for n in REP_NAMES)
    packs = [_pack([inp[prefix + n] for n in tiny_names]) for prefix in ('', 'm_', 'v_')]
    g_pack = _pack([misc_sum.reshape(-1)[:rep_total]] + [grads[n] for n in SMALL_NAMES])
    tiny_out = [_unpack(o, [inp[n].shape for n in tiny_names])
                for o in _adamw(packs[0], g_pack, packs[1], packs[2], "adamw_small")]
    for i, n in enumerate(tiny_names):
        upd[n] = tuple(o[i] for o in tiny_out)

    outs = [loss, grad_x[None]]
    outs += [grads[n] for n in WEIGHT_NAMES]
    for i in range(3):
        outs += [upd[n][i] for n in WEIGHT_NAMES]
    return tuple(outs)


def _step(inp):
    loss_part, grad_x, riders, rep_grads, small_grads = _local_step(inp)
    return _update(inp, loss_part, grad_x, riders, rep_grads, small_grads)


INPUT_NAMES = (['x', 'mem'] + WEIGHT_NAMES + ['loss_target'] + ['m_' + n for n in WEIGHT_NAMES]
               + ['v_' + n for n in WEIGHT_NAMES])


def kernel(x, mem, norm_mix_g, norm_xa_g, norm_ffn_g, norm_mem_g, norm_final_g, w_in_ab, conv_qkv_a, a_log_a, dt_bias_a, onorm_g_a, ssm_lambda_re, ssm_lambda_im, ssm_b_re, ssm_b_im, ssm_c_re, ssm_c_im, ssm_d, ssm_log_dt, w_glu_b, b_glu_b, w_out_ab, pool_w, pool_scale, xa_wq, xa_wkv, xa_wo, ffn_w_up, ffn_conv, ffn_w_down, loss_target, m_norm_mix_g, m_norm_xa_g, m_norm_ffn_g, m_norm_mem_g, m_norm_final_g, m_w_in_ab, m_conv_qkv_a, m_a_log_a, m_dt_bias_a, m_onorm_g_a, m_ssm_lambda_re, m_ssm_lambda_im, m_ssm_b_re, m_ssm_b_im, m_ssm_c_re, m_ssm_c_im, m_ssm_d, m_ssm_log_dt, m_w_glu_b, m_b_glu_b, m_w_out_ab, m_pool_w, m_pool_scale, m_xa_wq, m_xa_wkv, m_xa_wo, m_ffn_w_up, m_ffn_conv, m_ffn_w_down, v_norm_mix_g, v_norm_xa_g, v_norm_ffn_g, v_norm_mem_g, v_norm_final_g, v_w_in_ab, v_conv_qkv_a, v_a_log_a, v_dt_bias_a, v_onorm_g_a, v_ssm_lambda_re, v_ssm_lambda_im, v_ssm_b_re, v_ssm_b_im, v_ssm_c_re, v_ssm_c_im, v_ssm_d, v_ssm_log_dt, v_w_glu_b, v_b_glu_b, v_w_out_ab, v_pool_w, v_pool_scale, v_xa_wq, v_xa_wkv, v_xa_wo, v_ffn_w_up, v_ffn_conv, v_ffn_w_down):
    args = (x, mem, norm_mix_g, norm_xa_g, norm_ffn_g, norm_mem_g, norm_final_g, w_in_ab, conv_qkv_a, a_log_a, dt_bias_a, onorm_g_a, ssm_lambda_re, ssm_lambda_im, ssm_b_re, ssm_b_im, ssm_c_re, ssm_c_im, ssm_d, ssm_log_dt, w_glu_b, b_glu_b, w_out_ab, pool_w, pool_scale, xa_wq, xa_wkv, xa_wo, ffn_w_up, ffn_conv, ffn_w_down, loss_target, m_norm_mix_g, m_norm_xa_g, m_norm_ffn_g, m_norm_mem_g, m_norm_final_g, m_w_in_ab, m_conv_qkv_a, m_a_log_a, m_dt_bias_a, m_onorm_g_a, m_ssm_lambda_re, m_ssm_lambda_im, m_ssm_b_re, m_ssm_b_im, m_ssm_c_re, m_ssm_c_im, m_ssm_d, m_ssm_log_dt, m_w_glu_b, m_b_glu_b, m_w_out_ab, m_pool_w, m_pool_scale, m_xa_wq, m_xa_wkv, m_xa_wo, m_ffn_w_up, m_ffn_conv, m_ffn_w_down, v_norm_mix_g, v_norm_xa_g, v_norm_ffn_g, v_norm_mem_g, v_norm_final_g, v_w_in_ab, v_conv_qkv_a, v_a_log_a, v_dt_bias_a, v_onorm_g_a, v_ssm_lambda_re, v_ssm_lambda_im, v_ssm_b_re, v_ssm_b_im, v_ssm_c_re, v_ssm_c_im, v_ssm_d, v_ssm_log_dt, v_w_glu_b, v_b_glu_b, v_w_out_ab, v_pool_w, v_pool_scale, v_xa_wq, v_xa_wkv, v_xa_wo, v_ffn_w_up, v_ffn_conv, v_ffn_w_down)
    return _step(dict(zip(INPUT_NAMES, args)))
```

```python
import functools
import math

import numpy as np
import jax
import jax.numpy as jnp
from jax import lax
from jax.experimental import pallas as pl
from jax.experimental.pallas import tpu as pltpu
from jax.experimental.pallas import tpu_sc as plsc

F32, BF16 = jnp.float32, jnp.bfloat16
HIGH, HIGHEST = lax.Precision.HIGH, lax.Precision.HIGHEST
MESH = pl.DeviceIdType.MESH

N_DEV = 8
SEQ, D_MODEL, MEM_LEN = 2048, 1024, 256
WIDTH_A, N_HEADS_A, HEAD_A, CONV_A = 512, 4, 128, 4
GDR_CHUNK = 128
SSM_WIDTH, SSM_GROUP, N_GROUPS, SSM_STATE = 512, 16, 32, 64
SSM_CH = N_GROUPS * SSM_STATE
SCAN_CB = 512
POOL_WINDOWS = (2, 4, 8, 16)
POOL_GROUP = 256
N_HEADS_X, HEAD_X = 4, 256
D_FF, CONV_FFN = 2816, 3
RMS_EPS = 1e-6
ADAM_LR, ADAM_B1, ADAM_B2, ADAM_EPS, ADAM_WD, ADAM_STEP = 0.001, 0.9, 0.999, 1e-08, 0.01, 10
LANE = 128
PACK_COLS = 1024
VMEM_LIMIT_BYTES = 56 * 1024 * 1024


def _params(sem=None):
    return pltpu.CompilerParams(dimension_semantics=sem, vmem_limit_bytes=VMEM_LIMIT_BYTES)


class Comm:
    def __init__(self, inputs, out_shapes, sems, start, end, mid=None):
        self.inputs, self.out_shapes, self.sems = list(inputs), list(out_shapes), list(sems)
        self.start, self.mid, self.end = start, mid, end


def _merge_comms(comms):
    comms = [c for c in comms if c is not None]
    if not comms:
        return None, []
    bounds, ni, no, ns = [], 0, 0, 0
    for c in comms:
        bounds.append((ni, no, ns))
        ni, no, ns = ni + len(c.inputs), no + len(c.out_shapes), ns + len(c.sems)

    def phase(which):
        def run(ins, outs, sems):
            for c, (i0, o0, s0) in zip(comms, bounds):
                fn = getattr(c, which)
                if fn is not None:
                    fn(ins[i0:i0 + len(c.inputs)], outs[o0:o0 + len(c.out_shapes)], sems[s0:s0 + len(c.sems)])
        return run

    merged = Comm([a for c in comms for a in c.inputs], [s for c in comms for s in c.out_shapes],
                  [s for c in comms for s in c.sems], phase("start"), phase("end"), phase("mid"))
    return merged, [(o0, o0 + len(c.out_shapes)) for c, (_, o0, _) in zip(comms, bounds)]


def _call(body, *, name, grid, in_specs, out_specs, out_shape, args, scratch_shapes=(), sem=None, comm=None):
    single = not isinstance(out_shape, (list, tuple))
    out_specs_l = [out_specs] if single else list(out_specs)
    out_shape_l = [out_shape] if single else list(out_shape)
    scratch_shapes = list(scratch_shapes)
    merged, spans = _merge_comms(comm if isinstance(comm, (list, tuple)) else [comm])
    if merged is None:
        outs = pl.pallas_call(body, name=name, grid=grid, in_specs=list(in_specs), out_specs=out_specs_l,
                              out_shape=out_shape_l, scratch_shapes=scratch_shapes, compiler_params=_params(sem))(*args)
        outs = outs[0] if single else outs
        return outs if comm is None else (outs, [])
    n_in, n_out, n_scr = len(in_specs), len(out_specs_l), len(scratch_shapes)
    ci, co = len(merged.inputs), len(merged.out_shapes)
    total = int(np.prod(grid))

    def wrapped(*refs):
        ins, cins = refs[:n_in], refs[n_in:n_in + ci]
        outs, couts = refs[n_in + ci:n_in + ci + n_out], refs[n_in + ci + n_out:n_in + ci + n_out + co]
        scr, csems = refs[n_in + ci + n_out + co:n_in + ci + n_out + co + n_scr], refs[n_in + ci + n_out + co + n_scr:]
        lin = pl.program_id(0)
        for d in range(1, len(grid)):
            lin = lin * grid[d] + pl.program_id(d)
        pl.when(lin == 0)(lambda: merged.start(cins, couts, csems))
        body(*ins, *outs, *scr)
        mid_step = min((3 * total) // 4, total - 1)
        pl.when(lin == mid_step)(lambda: merged.mid(cins, couts, csems))
        pl.when(lin == total - 1)(lambda: merged.end(cins, couts, csems))

    any_spec = pl.BlockSpec(memory_space=pl.ANY)
    res = pl.pallas_call(
        wrapped, name=name, grid=grid, in_specs=list(in_specs) + [any_spec] * ci,
        out_specs=out_specs_l + [any_spec] * co, out_shape=out_shape_l + merged.out_shapes,
        scratch_shapes=scratch_shapes + merged.sems,
        compiler_params=_params(("arbitrary",) * len(grid)))(*args, *merged.inputs)
    outs, couts = res[:n_out], res[n_out:]
    return (outs[0] if single else list(outs)), [list(couts[a:b]) for a, b in spans]


def _comm_only(comm, name):
    def body():
        pass

    _, couts = _call(body, name=name, grid=(1,), in_specs=[], out_specs=[], out_shape=[], args=[], comm=comm)
    return couts[0]


def _tile(dim, pref):
    best = None
    for t in range(LANE, min(dim, pref) + 1, LANE):
        if dim % t == 0:
            best = t
    return best if best is not None else dim


MM_VMEM_BUDGET = 40 * 1024 * 1024


def _mm_tiles(m, n, k, a_bytes, b_bytes, o_bytes, r_bytes):
    for tk in (k, _tile(k, 2048), _tile(k, 1024), _tile(k, 512)):
        for tm, tn in ((1024, 1536), (1024, 1024), (1024, 512), (512, 512), (256, 512), (256, 256)):
            tm, tn = _tile(m, tm), _tile(n, tn)
            acc = 0 if tk == k else tm * tn * 4
            need = 2 * (tm * tk * a_bytes + tk * tn * b_bytes + tm * tn * (o_bytes + r_bytes)) + acc
            if need <= MM_VMEM_BUDGET:
                return tm, tn, tk
    raise ValueError("no matmul tiling fits VMEM")


def _mm(a, b, mode, name, out_dtype=F32, res=None, comm=None):
    if mode == "nn":
        (m, k), n = a.shape, b.shape[1]
    elif mode == "nt":
        (m, k), n = a.shape, b.shape[0]
    else:
        (k, m), n = a.shape, b.shape[1]
    tm, tn, tk = _mm_tiles(m, n, k, a.dtype.itemsize, b.dtype.itemsize, jnp.dtype(out_dtype).itemsize,
                           0 if res is None else res.dtype.itemsize)
    nk = k // tk
    dims = {"nn": ((1,), (0,)), "nt": ((1,), (1,)), "tn": ((0,), (0,))}[mode]

    def body(*refs):
        if res is None:
            a_ref, b_ref, o_ref = refs[:3]
            r_ref = None
        else:
            a_ref, b_ref, r_ref, o_ref = refs[:4]
        part = lax.dot_general(a_ref[...].astype(BF16), b_ref[...].astype(BF16), (dims, ((), ())),
                               preferred_element_type=F32)

        def finish(out):
            if r_ref is not None:
                out = out + r_ref[...].astype(F32)
            o_ref[...] = out.astype(out_dtype)

        if nk == 1:
            finish(part)
            return
        acc = refs[-1]
        kk = pl.program_id(2)

        @pl.when(kk == 0)
        def _():
            acc[...] = part

        @pl.when(kk > 0)
        def _():
            acc[...] += part

        @pl.when(kk == nk - 1)
        def _():
            finish(acc[...])

    a_spec = (pl.BlockSpec((tk, tm), lambda i, j, q: (q, i)) if mode == "tn"
              else pl.BlockSpec((tm, tk), lambda i, j, q: (i, q)))
    b_spec = (pl.BlockSpec((tn, tk), lambda i, j, q: (j, q)) if mode == "nt"
              else pl.BlockSpec((tk, tn), lambda i, j, q: (q, j)))
    o_spec = pl.BlockSpec((tm, tn), lambda i, j, q: (i, j))
    in_specs, args = [a_spec, b_spec], [a, b]
    if res is not None:
        in_specs.append(o_spec)
        args.append(res)
    return _call(body, name=name, grid=(m // tm, n // tn, nk), in_specs=in_specs, out_specs=o_spec,
                 out_shape=jax.ShapeDtypeStruct((m, n), out_dtype),
                 scratch_shapes=[] if nk == 1 else [pltpu.VMEM((tm, tn), F32)],
                 sem=("parallel", "parallel", "arbitrary"), args=args, comm=comm)


def _mm_parts(parts, b, mode, name, out_dtype=F32):
    count = len(parts)
    rows, cols = parts[0].shape
    n = b.shape[1]
    if mode == "nn":
        tm, tn, tk = _mm_tiles(rows, n, count * cols, parts[0].dtype.itemsize, b.dtype.itemsize,
                               jnp.dtype(out_dtype).itemsize, 0)
        assert tk == count * cols

        def body(*refs):
            a_refs, b_refs, o_ref = refs[:count], refs[count:2 * count], refs[2 * count]
            out = None
            for a_ref, b_ref in zip(a_refs, b_refs):
                part = jnp.dot(a_ref[...].astype(BF16), b_ref[...].astype(BF16), preferred_element_type=F32)
                out = part if out is None else out + part
            o_ref[...] = out.astype(out_dtype)

        return _call(body, name=name, grid=(rows // tm, n // tn),
                     in_specs=[pl.BlockSpec((tm, cols), lambda i, j: (i, 0))] * count
                     + [pl.BlockSpec((cols, tn), functools.partial(lambda q, i, j: (q, j), q)) for q in range(count)],
                     out_specs=pl.BlockSpec((tm, tn), lambda i, j: (i, j)),
                     out_shape=jax.ShapeDtypeStruct((rows, n), out_dtype),
                     sem=("parallel", "parallel"), args=(*parts, *([b] * count)))
    tm, tn = _tile(cols, 1536), _tile(n, 1024)
    per = cols // tm

    def body_t(*refs):
        a_refs, b_ref, o_ref = refs[:count], refs[count], refs[count + 1]
        for q, a_ref in enumerate(a_refs):
            @pl.when(pl.program_id(0) // per == q)
            def _(a_ref=a_ref):
                o_ref[...] = lax.dot_general(a_ref[...].astype(BF16), b_ref[...].astype(BF16),
                                             (((0,), (0,)), ((), ())), preferred_element_type=F32).astype(out_dtype)

    return _call(body_t, name=name, grid=(count * per, n // tn),
                 in_specs=[pl.BlockSpec((rows, tm), functools.partial(lambda q, i, j: (0, jnp.clip(i - q * per, 0, per - 1)), q))
                           for q in range(count)] + [pl.BlockSpec((rows, tn), lambda i, j: (0, j))],
                 out_specs=pl.BlockSpec((tm, tn), lambda i, j: (i, j)),
                 out_shape=jax.ShapeDtypeStruct((count * cols, n), out_dtype),
                 sem=("parallel", "parallel"), args=(*parts, b))


def _mm_bd(a, b, mode, name, out_dtype=F32, res=None, comm=None, tm=1024):
    if mode == "tn":
        k = a.shape[0]
        nb = min(a.shape[1], b.shape[1]) // LANE
        ma, n = a.shape[1] // nb, b.shape[1] // nb

        def body(a_ref, b_ref, o_ref):
            o_ref[0] = lax.dot_general(a_ref[...].astype(BF16), b_ref[...].astype(BF16), (((0,), (0,)), ((), ())),
                                       preferred_element_type=F32).astype(out_dtype)

        return _call(body, name=name, grid=(nb,),
                     in_specs=[pl.BlockSpec((k, ma), lambda j: (0, j)), pl.BlockSpec((k, n), lambda j: (0, j))],
                     out_specs=pl.BlockSpec((1, ma, n), lambda j: (j, 0, 0)),
                     out_shape=jax.ShapeDtypeStruct((nb, ma, n), out_dtype), sem=("parallel",), args=(a, b), comm=comm)
    m = a.shape[0]
    nb = b.shape[0]
    ka = a.shape[1] // nb
    n = b.shape[2] if mode == "nn" else b.shape[1]
    tm = _tile(m, tm)
    dims = ((1,), (0,)) if mode == "nn" else ((1,), (1,))

    def body(*refs):
        if res is None:
            a_ref, b_ref, o_ref = refs
            r_ref = None
        else:
            a_ref, b_ref, r_ref, o_ref = refs
        out = lax.dot_general(a_ref[...].astype(BF16), b_ref[0].astype(BF16), (dims, ((), ())),
                              preferred_element_type=F32)
        if r_ref is not None:
            out = out + r_ref[...].astype(F32)
        o_ref[...] = out.astype(out_dtype)

    o_spec = pl.BlockSpec((tm, n), lambda i, j: (i, j))
    in_specs = [pl.BlockSpec((tm, ka), lambda i, j: (i, j)), pl.BlockSpec((1,) + b.shape[1:], lambda i, j: (j, 0, 0))]
    args = [a, b]
    if res is not None:
        in_specs.append(o_spec)
        args.append(res)
    return _call(body, name=name, grid=(m // tm, nb), in_specs=in_specs, out_specs=o_spec,
                 out_shape=jax.ShapeDtypeStruct((m, nb * n), out_dtype), sem=("parallel", "parallel"),
                 args=args, comm=comm)


def _rms_fwd(x, g, out_dtype, name, tr=512):
    rows, d = x.shape
    tr = min(tr, rows)

    def body(x_ref, g_ref, o_ref):
        xv = x_ref[...]
        r = lax.rsqrt(jnp.mean(xv * xv, axis=-1, keepdims=True) + RMS_EPS)
        o_ref[...] = (xv * r * g_ref[...]).astype(out_dtype)

    return pl.pallas_call(
        body, name=name, grid=(rows // tr,),
        in_specs=[pl.BlockSpec((tr, d), lambda i: (i, 0)), pl.BlockSpec((1, d), lambda i: (0, 0))],
        out_specs=pl.BlockSpec((tr, d), lambda i: (i, 0)), out_shape=jax.ShapeDtypeStruct((rows, d), out_dtype),
        compiler_params=_params(("parallel",)))(x, g.reshape(1, d))


def _rms_bwd(x, g, dy, dres, name, tr=512, comm=None):
    rows, d = x.shape
    tr = min(tr, rows)

    def body(*refs):
        if dres is None:
            x_ref, g_ref, dy_ref, dx_ref, dg_ref = refs
            r_ref = None
        else:
            x_ref, g_ref, dy_ref, r_ref, dx_ref, dg_ref = refs

        @pl.when(pl.program_id(0) == 0)
        def _():
            dg_ref[...] = jnp.zeros_like(dg_ref)

        xv, dyv = x_ref[...], dy_ref[...].astype(F32)
        r = lax.rsqrt(jnp.mean(xv * xv, axis=-1, keepdims=True) + RMS_EPS)
        xh = xv * r
        dyg = dyv * g_ref[...]
        dx = r * (dyg - xh * jnp.mean(dyg * xh, axis=-1, keepdims=True))
        if r_ref is not None:
            dx = dx + r_ref[...]
        dx_ref[...] = dx
        dg_ref[...] += jnp.sum(dyv * xh, axis=0, keepdims=True)

    blk = pl.BlockSpec((tr, d), lambda i: (i, 0))
    vec = pl.BlockSpec((1, d), lambda i: (0, 0))
    in_specs, args = [blk, vec, blk], [x, g.reshape(1, d), dy]
    if dres is not None:
        in_specs.append(blk)
        args.append(dres)
    return _call(
        body, name=name, grid=(rows // tr,), in_specs=in_specs, out_specs=[blk, vec],
        out_shape=[jax.ShapeDtypeStruct((rows, d), F32), jax.ShapeDtypeStruct((1, d), F32)],
        sem=("arbitrary",), args=args, comm=comm)


def _loss_head(x, g, target, name, tr=512):
    rows, d = x.shape

    def body(x_ref, g_ref, t_ref, loss_ref, dx_ref, dg_ref):
        @pl.when(pl.program_id(0) == 0)
        def _():
            dg_ref[...] = jnp.zeros_like(dg_ref)
            loss_ref[...] = jnp.zeros_like(loss_ref)

        xv = x_ref[...]
        r = lax.rsqrt(jnp.mean(xv * xv, axis=-1, keepdims=True) + RMS_EPS)
        xh = xv * r
        err = xh * g_ref[...] - t_ref[...]
        loss_ref[...] += 0.5 * jnp.sum(jnp.mean(err * err, axis=-1, keepdims=True), keepdims=True)
        dyv = err * (1.0 / d)
        dyg = dyv * g_ref[...]
        dx_ref[...] = r * (dyg - xh * jnp.mean(dyg * xh, axis=-1, keepdims=True))
        dg_ref[...] += jnp.sum(dyv * xh, axis=0, keepdims=True)

    blk = pl.BlockSpec((tr, d), lambda i: (i, 0))
    vec = pl.BlockSpec((1, d), lambda i: (0, 0))
    return pl.pallas_call(
        body, name=name, grid=(rows // tr,), in_specs=[blk, vec, blk],
        out_specs=[pl.BlockSpec((1, 1), lambda i: (0, 0)), blk, vec],
        out_shape=[jax.ShapeDtypeStruct((1, 1), F32), jax.ShapeDtypeStruct((rows, d), F32),
                   jax.ShapeDtypeStruct((1, d), F32)],
        compiler_params=_params(("arbitrary",)))(x, g.reshape(1, d), target)


def _shift_down(x, s):
    rows = lax.broadcasted_iota(jnp.int32, x.shape, 0)
    return jnp.where(rows >= s, pltpu.roll(x, s, 0), 0.0)


def _shift_up(x, s):
    n = x.shape[0]
    rows = lax.broadcasted_iota(jnp.int32, x.shape, 0)
    return jnp.where(rows < n - s, pltpu.roll(x, n - s, 0), 0.0)


def _sigmoid(x):
    return 1.0 / (1.0 + jnp.exp(-x))


def _silu_and_grad(x):
    s = _sigmoid(x)
    return x * s, s * (1.0 + x * (1.0 - s))


_GELU_C0, _GELU_C1 = math.sqrt(2.0 / math.pi), 0.044715


def _gelu_and_grad(x):
    th = jnp.tanh(_GELU_C0 * (x + _GELU_C1 * x * x * x))
    y = 0.5 * x * (1.0 + th)
    dy = 0.5 * (1.0 + th) + 0.5 * x * (1.0 - th * th) * _GELU_C0 * (1.0 + 3.0 * _GELU_C1 * x * x)
    return y, dy


def _ffn_act_fwd(h, w, name, tc=256, comm=None):
    t = h.shape[0]
    nb = D_FF // tc

    def body(hg_ref, hv_ref, wg_ref, wv_ref, a_ref):
        def conv(x, wr):
            return wr[2:3, :] * x + wr[1:2, :] * _shift_down(x, 1) + wr[0:1, :] * _shift_down(x, 2)

        cg = conv(hg_ref[...], wg_ref[...])
        cv = conv(hv_ref[...], wv_ref[...])
        a_ref[...] = (cg * _sigmoid(cg) * cv).astype(BF16)

    return _call(
        body, name=name, grid=(nb,),
        in_specs=[pl.BlockSpec((t, tc), lambda j: (0, j)), pl.BlockSpec((t, tc), lambda j: (0, j + nb)),
                  pl.BlockSpec((CONV_FFN, tc), lambda j: (0, j)), pl.BlockSpec((CONV_FFN, tc), lambda j: (0, j + nb))],
        out_specs=pl.BlockSpec((t, tc), lambda j: (0, j)), out_shape=jax.ShapeDtypeStruct((t, D_FF), BF16),
        sem=("parallel",), args=(h, h, w, w), comm=comm)


def _ffn_act_bwd(h, w, da, name, tc=256, comm=None):
    t = h.shape[0]
    nb = D_FF // tc

    def body(hg_ref, hv_ref, wg_ref, wv_ref, da_ref, dhg_ref, dhv_ref, dwg_ref, dwv_ref):
        hg, hv, wg, wv = hg_ref[...], hv_ref[...], wg_ref[...], wv_ref[...]
        hg1, hg2, hv1, hv2 = _shift_down(hg, 1), _shift_down(hg, 2), _shift_down(hv, 1), _shift_down(hv, 2)
        cg = wg[2:3, :] * hg + wg[1:2, :] * hg1 + wg[0:1, :] * hg2
        cv = wv[2:3, :] * hv + wv[1:2, :] * hv1 + wv[0:1, :] * hv2
        sg, dsg = _silu_and_grad(cg)
        dav = da_ref[...].astype(F32)
        dcv = dav * sg
        dcg = dav * cv * dsg

        def conv_t(dc, wr):
            return wr[2:3, :] * dc + wr[1:2, :] * _shift_up(dc, 1) + wr[0:1, :] * _shift_up(dc, 2)

        dhg_ref[...] = conv_t(dcg, wg).astype(BF16)
        dhv_ref[...] = conv_t(dcv, wv).astype(BF16)
        dwg_ref[0:1, :] = jnp.sum(dcg * hg2, axis=0, keepdims=True)
        dwg_ref[1:2, :] = jnp.sum(dcg * hg1, axis=0, keepdims=True)
        dwg_ref[2:3, :] = jnp.sum(dcg * hg, axis=0, keepdims=True)
        dwv_ref[0:1, :] = jnp.sum(dcv * hv2, axis=0, keepdims=True)
        dwv_ref[1:2, :] = jnp.sum(dcv * hv1, axis=0, keepdims=True)
        dwv_ref[2:3, :] = jnp.sum(dcv * hv, axis=0, keepdims=True)

    big = lambda off: pl.BlockSpec((t, tc), lambda j: (0, j + off))
    small = lambda off: pl.BlockSpec((CONV_FFN, tc), lambda j: (0, j + off))
    res = _call(
        body, name=name, grid=(nb,),
        in_specs=[big(0), big(nb), small(0), small(nb), big(0)],
        out_specs=[big(0), big(0), small(0), small(0)],
        out_shape=[jax.ShapeDtypeStruct((t, D_FF), BF16), jax.ShapeDtypeStruct((t, D_FF), BF16),
                   jax.ShapeDtypeStruct((CONV_FFN, D_FF), F32), jax.ShapeDtypeStruct((CONV_FFN, D_FF), F32)],
        sem=("parallel",), args=(h, h, w, w, da), comm=comm)
    (dhg, dhv, dwg, dwv), couts = res if comm is not None else (res, None)
    out = ((dhg, dhv), jnp.concatenate([dwg, dwv], axis=1))
    return out if comm is None else (out, couts)


def _attn_probs(q, k):
    s = lax.dot_general(q.astype(BF16), k.astype(BF16), (((1,), (1,)), ((), ())),
                        preferred_element_type=F32) * (HEAD_X ** -0.5)
    s = s - jnp.max(s, axis=-1, keepdims=True)
    p = jnp.exp(s)
    return p / jnp.sum(p, axis=-1, keepdims=True)


def _attn_fwd(q, kv, name, tq=512, comm=None):
    t = q.shape[0]

    def body(q_ref, k_ref, v_ref, o_ref):
        p = _attn_probs(q_ref[...], k_ref[...])
        o_ref[...] = jnp.dot(p.astype(BF16), v_ref[...].astype(BF16), preferred_element_type=F32).astype(BF16)

    return _call(
        body, name=name, grid=(N_HEADS_X, t // tq),
        in_specs=[pl.BlockSpec((tq, HEAD_X), lambda h, i: (i, h)),
                  pl.BlockSpec((MEM_LEN, HEAD_X), lambda h, i: (0, h)),
                  pl.BlockSpec((MEM_LEN, HEAD_X), lambda h, i: (0, h + N_HEADS_X))],
        out_specs=pl.BlockSpec((tq, HEAD_X), lambda h, i: (i, h)),
        out_shape=jax.ShapeDtypeStruct((t, N_HEADS_X * HEAD_X), BF16),
        sem=("parallel", "parallel"), args=(q, kv, kv), comm=comm)


def _attn_bwd(q, kv, do, name, tq=512):
    t = q.shape[0]

    def body(q_ref, k_ref, v_ref, do_ref, dq_ref, dk_ref, dv_ref):
        @pl.when(pl.program_id(1) == 0)
        def _():
            dk_ref[...] = jnp.zeros_like(dk_ref)
            dv_ref[...] = jnp.zeros_like(dv_ref)

        qb, kb, vb, dob = (r[...].astype(BF16) for r in (q_ref, k_ref, v_ref, do_ref))
        p = _attn_probs(qb, kb)
        dp = lax.dot_general(dob, vb, (((1,), (1,)), ((), ())), preferred_element_type=F32)
        ds = p * (dp - jnp.sum(dp * p, axis=-1, keepdims=True)) * (HEAD_X ** -0.5)
        dsb = ds.astype(BF16)
        dq_ref[...] = jnp.dot(dsb, kb, preferred_element_type=F32).astype(BF16)
        dk_ref[...] += lax.dot_general(dsb, qb, (((0,), (0,)), ((), ())), preferred_element_type=F32)
        dv_ref[...] += lax.dot_general(p.astype(BF16), dob, (((0,), (0,)), ((), ())), preferred_element_type=F32)

    qs = pl.BlockSpec((tq, HEAD_X), lambda h, i: (i, h))
    ms = pl.BlockSpec((MEM_LEN, HEAD_X), lambda h, i: (0, h))
    return pl.pallas_call(
        body, name=name, grid=(N_HEADS_X, t // tq),
        in_specs=[qs, ms, pl.BlockSpec((MEM_LEN, HEAD_X), lambda h, i: (0, h + N_HEADS_X)), qs],
        out_specs=[qs, ms, ms],
        out_shape=[jax.ShapeDtypeStruct((t, D_MODEL), BF16), jax.ShapeDtypeStruct((MEM_LEN, D_MODEL), F32),
                   jax.ShapeDtypeStruct((MEM_LEN, D_MODEL), F32)],
        compiler_params=_params(("parallel", "arbitrary")))(q, kv, kv, do)


def _pool_counts(t, win):
    pos = lax.broadcasted_iota(jnp.int32, (t, 1), 0).astype(F32) + 1.0
    return 1.0 / jnp.minimum(pos, float(win))


def _pool_delta(xv, win):
    s, step = xv, 1
    while step < win:
        s = s + _shift_down(s, step)
        step *= 2
    return s * _pool_counts(xv.shape[0], win) - xv


def _pool_delta_t(dv, win):
    s, step = dv * _pool_counts(dv.shape[0], win), 1
    while step < win:
        s = s + _shift_up(s, step)
        step *= 2
    return s - dv


def _pool_fwd(xn, w, scale, res, name):
    t = xn.shape[0]

    def make_branch(win, xn_ref, w_ref, s_ref, r_ref, o_ref):
        def branch():
            dl = _pool_delta(xn_ref[...], win)
            y = jnp.dot(dl.astype(BF16), w_ref[0], preferred_element_type=F32)
            o_ref[...] = r_ref[...] + y * s_ref[...]
        return branch

    def body(xn_ref, w_ref, s_ref, r_ref, o_ref):
        for gi, win in enumerate(POOL_WINDOWS):
            pl.when(pl.program_id(0) == gi)(make_branch(win, xn_ref, w_ref, s_ref, r_ref, o_ref))

    blk = pl.BlockSpec((t, POOL_GROUP), lambda g: (0, g))
    return pl.pallas_call(
        body, name=name, grid=(len(POOL_WINDOWS),),
        in_specs=[blk, pl.BlockSpec((1, POOL_GROUP, POOL_GROUP), lambda g: (g, 0, 0)),
                  pl.BlockSpec((1, POOL_GROUP), lambda g: (0, g)), blk],
        out_specs=blk, out_shape=jax.ShapeDtypeStruct((t, D_MODEL), F32),
        compiler_params=_params(("parallel",)))(xn, w, scale, res)


def _pool_bwd(xn, w, scale, dmix, name):
    t = xn.shape[0]

    def make_branch(win, xn_ref, w_ref, s_ref, d_ref, dxn_ref, dw_ref, ds_ref):
        def branch():
            dl = _pool_delta(xn_ref[...], win).astype(BF16)
            wv = w_ref[0]
            dm = d_ref[...]
            y = jnp.dot(dl, wv, preferred_element_type=F32)
            ds_ref[...] = jnp.sum(dm * y, axis=0, keepdims=True)
            dy = (dm * s_ref[...]).astype(BF16)
            dw_ref[0] = lax.dot_general(dl, dy, (((0,), (0,)), ((), ())), preferred_element_type=F32)
            ddl = lax.dot_general(dy, wv, (((1,), (1,)), ((), ())), preferred_element_type=F32)
            dxn_ref[...] = _pool_delta_t(ddl, win)
        return branch

    def body(*refs):
        for gi, win in enumerate(POOL_WINDOWS):
            pl.when(pl.program_id(0) == gi)(make_branch(win, *refs))

    blk = pl.BlockSpec((t, POOL_GROUP), lambda g: (0, g))
    wspec = pl.BlockSpec((1, POOL_GROUP, POOL_GROUP), lambda g: (g, 0, 0))
    vec = pl.BlockSpec((1, POOL_GROUP), lambda g: (0, g))
    return pl.pallas_call(
        body, name=name, grid=(len(POOL_WINDOWS),), in_specs=[blk, wspec, vec, blk], out_specs=[blk, wspec, vec],
        out_shape=[jax.ShapeDtypeStruct((t, D_MODEL), F32),
                   jax.ShapeDtypeStruct((len(POOL_WINDOWS), POOL_GROUP, POOL_GROUP), F32),
                   jax.ShapeDtypeStruct((1, D_MODEL), F32)],
        compiler_params=_params(("parallel",)))(xn, w, scale, dmix)


def _qkv_conv(h, wr):
    return (wr[3:4, :] * h + wr[2:3, :] * _shift_down(h, 1) + wr[1:2, :] * _shift_down(h, 2)
            + wr[0:1, :] * _shift_down(h, 3))


def _qkv_block_kind(j):
    return j < 2 * N_HEADS_A, jnp.where(j < N_HEADS_A, HEAD_A ** -0.5, 1.0)


def _qkv_pre_fwd(h, w, name):
    t, cols = h.shape

    def body(h_ref, w_ref, o_ref):
        normalised, scale = _qkv_block_kind(pl.program_id(0))
        c = _qkv_conv(h_ref[...], w_ref[...])
        s = c * _sigmoid(c)
        r = lax.rsqrt(jnp.sum(s * s, axis=-1, keepdims=True) + 1e-6)
        o_ref[...] = jnp.where(normalised, s * (r * scale), s)

    blk = pl.BlockSpec((t, HEAD_A), lambda j: (0, j))
    return pl.pallas_call(
        body, name=name, grid=(cols // HEAD_A,), in_specs=[blk, pl.BlockSpec((CONV_A, HEAD_A), lambda j: (0, j))],
        out_specs=blk, out_shape=jax.ShapeDtypeStruct((t, cols), F32), compiler_params=_params(("parallel",)))(h, w)


def _qkv_pre_bwd(h, w, dy, name):
    t, cols = h.shape

    def body(h_ref, w_ref, dy_ref, dh_ref, dw_ref):
        normalised, scale = _qkv_block_kind(pl.program_id(0))
        hv, wr, dyv = h_ref[...], w_ref[...], dy_ref[...]
        h1, h2, h3 = _shift_down(hv, 1), _shift_down(hv, 2), _shift_down(hv, 3)
        c = wr[3:4, :] * hv + wr[2:3, :] * h1 + wr[1:2, :] * h2 + wr[0:1, :] * h3
        s, dsilu = _silu_and_grad(c)
        r = lax.rsqrt(jnp.sum(s * s, axis=-1, keepdims=True) + 1e-6)
        y = s * r
        dys = dyv * scale
        ds = jnp.where(normalised, r * (dys - y * jnp.sum(dys * y, axis=-1, keepdims=True)), dyv)
        dc = ds * dsilu
        dh = (wr[3:4, :] * dc + wr[2:3, :] * _shift_up(dc, 1) + wr[1:2, :] * _shift_up(dc, 2)
              + wr[0:1, :] * _shift_up(dc, 3))
        dh_ref[...] = dh.astype(BF16)
        dw_ref[0:1, :] = jnp.sum(dc * h3, axis=0, keepdims=True)
        dw_ref[1:2, :] = jnp.sum(dc * h2, axis=0, keepdims=True)
        dw_ref[2:3, :] = jnp.sum(dc * h1, axis=0, keepdims=True)
        dw_ref[3:4, :] = jnp.sum(dc * hv, axis=0, keepdims=True)

    blk = pl.BlockSpec((t, HEAD_A), lambda j: (0, j))
    taps = pl.BlockSpec((CONV_A, HEAD_A), lambda j: (0, j))
    return pl.pallas_call(
        body, name=name, grid=(cols // HEAD_A,), in_specs=[blk, taps, blk], out_specs=[blk, taps],
        out_shape=[jax.ShapeDtypeStruct((t, cols), BF16), jax.ShapeDtypeStruct((CONV_A, cols), F32)],
        compiler_params=_params(("parallel",)))(h, w, dy)


def _softplus(x):
    return jnp.maximum(x, 0.0) + jnp.log1p(jnp.exp(-jnp.abs(x)))


def _gates_fwd(ba, arow, brow, name):
    t = ba.shape[0]

    def body(x_ref, a_ref, b_ref, o_ref):
        xv = x_ref[...]
        lane = lax.broadcasted_iota(jnp.int32, xv.shape, 1)
        beta = _sigmoid(xv)
        g = -jnp.exp(a_ref[...]) * _softplus(xv + b_ref[...])
        o_ref[...] = jnp.where(lane < N_HEADS_A, beta, jnp.where(lane < 2 * N_HEADS_A, g, 0.0))

    return pl.pallas_call(body, name=name, out_shape=jax.ShapeDtypeStruct((t, LANE), F32),
                          compiler_params=_params())(ba, arow, brow)


def _gates_bwd(ba, arow, brow, dgb, name):
    t = ba.shape[0]

    def body(x_ref, a_ref, b_ref, d_ref, dx_ref, da_ref, db_ref):
        xv = x_ref[...]
        dv = d_ref[0] + d_ref[1] + d_ref[2] + d_ref[3]
        lane = lax.broadcasted_iota(jnp.int32, xv.shape, 1)
        beta = _sigmoid(xv)
        ea = jnp.exp(a_ref[...])
        z = xv + b_ref[...]
        dgv = jnp.where((lane >= N_HEADS_A) & (lane < 2 * N_HEADS_A), dv, 0.0) * (-ea)
        dz = dgv * _sigmoid(z)
        dx = jnp.where(lane < N_HEADS_A, dv * beta * (1.0 - beta), dz)
        dx_ref[...] = dx.astype(BF16)
        db_ref[...] = jnp.sum(dz, axis=0, keepdims=True)
        da_ref[...] = jnp.sum(dgv * _softplus(z), axis=0, keepdims=True)

    return pl.pallas_call(
        body, name=name,
        out_shape=[jax.ShapeDtypeStruct((t, LANE), BF16), jax.ShapeDtypeStruct((1, LANE), F32),
                   jax.ShapeDtypeStruct((1, LANE), F32)],
        compiler_params=_params())(ba, arow, brow, dgb)


def _head_gates(gates, head):
    lane = lax.broadcasted_iota(jnp.int32, gates.shape, 1)
    beta = jnp.sum(jnp.where(lane == head, gates, 0.0), axis=1, keepdims=True)
    g = jnp.sum(jnp.where(lane == head + N_HEADS_A, gates, 0.0), axis=1, keepdims=True)
    return beta, g


_B_NN, _B_NT, _B_TN = ((2,), (1,)), ((2,), (2,)), ((1,), (1,))


def _bdot(a, b, dims=_B_NN, prec=None):
    if prec is None:
        a, b = a.astype(BF16), b.astype(BF16)
    return lax.dot_general(a, b, (dims, ((0,), (0,))), precision=prec, preferred_element_type=F32)


def _heads_of(ref):
    return jnp.stack([ref[:, h * HEAD_A:(h + 1) * HEAD_A] for h in range(N_HEADS_A)])


def _all_head_gates(gates):
    pairs = [_head_gates(gates, h) for h in range(N_HEADS_A)]
    return jnp.stack([b for b, _ in pairs]), jnp.stack([g for _, g in pairs])


def _gdr_terms(k, beta, g):
    h, c = k.shape[0], GDR_CHUNK
    row = lax.broadcasted_iota(jnp.int32, (c, c), 0)
    col = lax.broadcasted_iota(jnp.int32, (c, c), 1)
    causal, strict = row >= col, row > col
    lower = jnp.broadcast_to(causal.astype(F32), (h, c, c))
    gcum = _bdot(lower, jnp.broadcast_to(g, (h, c, c)), prec=HIGHEST)
    diff = gcum - jnp.swapaxes(gcum, 1, 2)
    decay = jnp.where(causal, jnp.exp(jnp.where(causal, diff, 0.0)), 0.0)
    kb = k * beta
    return row, col, causal, strict, gcum, decay, kb, _bdot(kb, k, _B_NT)


def _unit_lower_inverses(a):
    c = a.shape[1]
    eye = (lax.broadcasted_iota(jnp.int32, (c, c), 0) == lax.broadcasted_iota(jnp.int32, (c, c), 1)).astype(F32)
    p = -a
    inv = eye + p
    step = 1
    while 2 * step < c:
        p = _bdot(p, p, prec=HIGH)
        inv = inv + _bdot(inv, p, prec=HIGH)
        step *= 2
    return inv


def _gdr_fwd(qkv, gates, name, comm=None):
    t = qkv.shape[0]
    c, nh = GDR_CHUNK, N_HEADS_A
    n = t // c

    def body(q_ref, k_ref, v_ref, gb_ref, o_ref, tm_ref, s_ref, state):
        @pl.when(pl.program_id(0) == 0)
        def _():
            state[...] = jnp.zeros_like(state)

        qv, kv, vv = _heads_of(q_ref), _heads_of(k_ref), _heads_of(v_ref)
        beta, g = _all_head_gates(gb_ref[...])
        row, col, causal, strict, gcum, decay, kb, kk = _gdr_terms(kv, beta, g)
        tm = _unit_lower_inverses(jnp.where(strict, kk * decay, 0.0))
        e = jnp.exp(gcum)
        u = _bdot(tm, vv * beta, prec=HIGH)
        w = _bdot(tm, kb * e, prec=HIGH)
        p = jnp.where(causal, _bdot(qv, kv, _B_NT) * decay, 0.0)
        s = state[...]
        s_ref[:, 0] = s
        tm_ref[:, 0] = tm
        vn = u - _bdot(w, s)
        o = _bdot(qv * e, s) + _bdot(p, vn)
        for h in range(nh):
            o_ref[:, h * HEAD_A:(h + 1) * HEAD_A] = o[h]
        glast = gcum[:, c - 1:c, :]
        state[...] = s * jnp.exp(glast) + _bdot(kv * jnp.exp(glast - gcum), vn, _B_TN)

    part = lambda p: pl.BlockSpec((c, WIDTH_A), lambda i: (i, p))
    mat = pl.BlockSpec((nh, 1, c, c), lambda i: (0, i, 0, 0))
    return _call(
        body, name=name, grid=(n,), in_specs=[part(0), part(1), part(2), pl.BlockSpec((c, LANE), lambda i: (i, 0))],
        out_specs=[part(0), mat, mat],
        out_shape=[jax.ShapeDtypeStruct((t, WIDTH_A), F32), jax.ShapeDtypeStruct((nh, n, c, c), F32),
                   jax.ShapeDtypeStruct((nh, n, HEAD_A, HEAD_A), F32)],
        scratch_shapes=[pltpu.VMEM((nh, HEAD_A, HEAD_A), F32)], sem=("arbitrary",),
        args=(qkv, qkv, qkv, gates), comm=comm)


def _gdr_bwd(qkv, gates, tm_all, s_all, do, name, comm=None):
    t = qkv.shape[0]
    c, nh = GDR_CHUNK, N_HEADS_A
    n = t // c

    def body(q_ref, k_ref, v_ref, gb_ref, tm_ref, s_ref, do_ref, dqkv_ref, dgb_ref, dstate):
        @pl.when(pl.program_id(0) == 0)
        def _():
            dstate[...] = jnp.zeros_like(dstate)

        qv, kv, vv, dov = _heads_of(q_ref), _heads_of(k_ref), _heads_of(v_ref), _heads_of(do_ref)
        beta, g = _all_head_gates(gb_ref[...])
        tm, s, dsp = tm_ref[:, 0], s_ref[:, 0], dstate[...]
        row, col, causal, strict, gcum, decay, kb, kk = _gdr_terms(kv, beta, g)
        rowsum = lambda x: jnp.sum(x, axis=2, keepdims=True)
        e = jnp.exp(gcum)
        vb, kbe = vv * beta, kb * e
        u = _bdot(tm, vb, prec=HIGH)
        w = _bdot(tm, kbe, prec=HIGH)
        qk = _bdot(qv, kv, _B_NT)
        p = jnp.where(causal, qk * decay, 0.0)
        vn = u - _bdot(w, s)
        glast = gcum[:, c - 1:c, :]
        el = jnp.exp(glast)
        f = jnp.exp(glast - gcum)
        kd = kv * f
        qe = qv * e

        dvn = _bdot(p, dov, _B_TN) + _bdot(kd, dsp)
        dglast = el[:, :, 0:1] * jnp.sum(s * dsp, axis=(1, 2), keepdims=True)
        dkd = _bdot(vn, dsp, _B_NT)
        dk = dkd * f
        df = rowsum(dkd * kv) * f[:, :, 0:1]
        dglast = dglast + jnp.sum(df, axis=1, keepdims=True)
        dgc = -df
        dp = jnp.where(causal, _bdot(dov, vn, _B_NT), 0.0)
        dqe = _bdot(dov, s, _B_NT)
        dq = dqe * e
        de = rowsum(dqe * qv)
        dstate[...] = dsp * el + _bdot(qe, dov, _B_TN) - _bdot(w, dvn, _B_TN)
        dw = -_bdot(dvn, s, _B_NT)
        dvb = _bdot(tm, dvn, _B_TN, prec=HIGH)
        dkbe = _bdot(tm, dw, _B_TN, prec=HIGH)
        da = -jnp.where(strict, _bdot(dvb, u, _B_NT) + _bdot(dkbe, w, _B_NT), 0.0)
        dkk = da * decay
        dqk = dp * decay
        dd = da * kk + dp * qk
        dq = dq + _bdot(dqk, kv)
        dk = dk + _bdot(dqk, qv, _B_TN)
        dkb = _bdot(dkk, kv) + dkbe * e
        dk = dk + _bdot(dkk, kb, _B_TN)
        de = de + rowsum(dkbe * kb)
        dk = dk + dkb * beta
        dbeta = rowsum(dkb * kv) + rowsum(dvb * vv)
        m = dd * decay
        dgc = dgc + rowsum(m) - rowsum(jnp.swapaxes(m, 1, 2))
        dgc = dgc + de * e[:, :, 0:1]
        dgc = dgc + jnp.where(row[:, 0:1] == c - 1, dglast, 0.0)
        upper = jnp.broadcast_to((row <= col).astype(F32), (nh, c, c))
        dg = _bdot(upper, jnp.broadcast_to(dgc, (nh, c, c)), prec=HIGHEST)
        dv = dvb * beta
        for p, grad in enumerate((dq, dk, dv)):
            for h in range(nh):
                dqkv_ref[:, p * WIDTH_A + h * HEAD_A:p * WIDTH_A + (h + 1) * HEAD_A] = grad[h]
        head = lax.broadcasted_iota(jnp.int32, (nh, c, LANE), 0)
        lane = lax.broadcasted_iota(jnp.int32, (nh, c, LANE), 2)
        dgb_ref[...] = jnp.where(lane == head, dbeta, jnp.where(lane == head + nh, dg, 0.0))

    part = lambda p: pl.BlockSpec((c, WIDTH_A), lambda i: (n - 1 - i, p))
    mat = pl.BlockSpec((nh, 1, c, c), lambda i: (0, n - 1 - i, 0, 0))
    return _call(
        body, name=name, grid=(n,),
        in_specs=[part(0), part(1), part(2), pl.BlockSpec((c, LANE), lambda i: (n - 1 - i, 0)), mat, mat, part(0)],
        out_specs=[pl.BlockSpec((c, 3 * WIDTH_A), lambda i: (n - 1 - i, 0)),
                   pl.BlockSpec((nh, c, LANE), lambda i: (0, n - 1 - i, 0))],
        out_shape=[jax.ShapeDtypeStruct((t, 3 * WIDTH_A), F32), jax.ShapeDtypeStruct((nh, t, LANE), F32)],
        scratch_shapes=[pltpu.VMEM((nh, HEAD_A, HEAD_A), F32)], sem=("arbitrary",),
        args=(qkv, qkv, qkv, gates, tm_all, s_all, do), comm=comm)


def _onorm_fwd(o, gate, g, name):
    t = o.shape[0]

    def body(o_ref, gate_ref, g_ref, y_ref):
        ov, gv = o_ref[...], gate_ref[...]
        r = lax.rsqrt(jnp.mean(ov * ov, axis=-1, keepdims=True) + RMS_EPS)
        y_ref[...] = (ov * r * g_ref[...] * gv * _sigmoid(gv)).astype(BF16)

    blk = pl.BlockSpec((t, HEAD_A), lambda j: (0, j))
    return pl.pallas_call(
        body, name=name, grid=(N_HEADS_A,), in_specs=[blk, blk, pl.BlockSpec((1, HEAD_A), lambda j: (0, 0))],
        out_specs=blk, out_shape=jax.ShapeDtypeStruct((t, WIDTH_A), BF16),
        compiler_params=_params(("parallel",)))(o, gate, g)


def _onorm_bwd(o, gate, g, dy, name):
    t = o.shape[0]

    def body(o_ref, gate_ref, g_ref, dy_ref, do_ref, dgate_ref, dg_ref):
        @pl.when(pl.program_id(0) == 0)
        def _():
            dg_ref[...] = jnp.zeros_like(dg_ref)

        ov, gv, dyv = o_ref[...], gate_ref[...], dy_ref[...].astype(F32)
        r = lax.rsqrt(jnp.mean(ov * ov, axis=-1, keepdims=True) + RMS_EPS)
        oh = ov * r
        sg, dsg = _silu_and_grad(gv)
        dgate_ref[...] = (dyv * oh * g_ref[...] * dsg).astype(BF16)
        dn = dyv * sg
        dg_ref[...] += jnp.sum(dn * oh, axis=0, keepdims=True)
        dng = dn * g_ref[...]
        do_ref[...] = r * (dng - oh * jnp.mean(dng * oh, axis=-1, keepdims=True))

    blk = pl.BlockSpec((t, HEAD_A), lambda j: (0, j))
    vec = pl.BlockSpec((1, HEAD_A), lambda j: (0, 0))
    return pl.pallas_call(
        body, name=name, grid=(N_HEADS_A,), in_specs=[blk, blk, vec, blk], out_specs=[blk, blk, vec],
        out_shape=[jax.ShapeDtypeStruct((t, WIDTH_A), F32), jax.ShapeDtypeStruct((t, WIDTH_A), BF16),
                   jax.ShapeDtypeStruct((1, HEAD_A), F32)],
        compiler_params=_params(("arbitrary",)))(o, gate, g, dy)


def _cmul(ar, ai, br, bi):
    return ar * br - ai * bi, ar * bi + ai * br


def _scan_tables(ar, ai, reverse):
    p1 = (ar, ai)
    p2 = _cmul(*p1, *p1)
    p4 = _cmul(*p2, *p2)
    p8 = _cmul(*p4, *p4)
    p3 = _cmul(*p2, *p1)
    p5 = _cmul(*p4, *p1)
    p6 = _cmul(*p4, *p2)
    p7 = _cmul(*p4, *p3)
    pows = [p1, p2, p3, p4, p5, p6, p7, p8]
    rows = lax.broadcasted_iota(jnp.int32, (8, ar.shape[1]), 0)
    tr = jnp.zeros((8, ar.shape[1]), F32)
    ti = jnp.zeros((8, ar.shape[1]), F32)
    for r in range(8):
        pw = pows[7 - r] if reverse else pows[r]
        tr = jnp.where(rows == r, pw[0], tr)
        ti = jnp.where(rows == r, pw[1], ti)
    return p1, p2, p4, p8, tr, ti


def _tile_scan(xr, xi, p1, p2, p4, reverse):
    rows = lax.broadcasted_iota(jnp.int32, xr.shape, 0)
    for s, (pr, pi) in ((1, p1), (2, p2), (4, p4)):
        if reverse:
            keep = rows < 8 - s
            sr, si = pltpu.roll(xr, 8 - s, 0), pltpu.roll(xi, 8 - s, 0)
        else:
            keep = rows >= s
            sr, si = pltpu.roll(xr, s, 0), pltpu.roll(xi, s, 0)
        sr, si = jnp.where(keep, sr, 0.0), jnp.where(keep, si, 0.0)
        mr, mi = _cmul(pr, pi, sr, si)
        xr, xi = xr + mr, xi + mi
    return xr, xi


def _s5_scan_fwd(bu, a, name, tb=512, comm=None):
    t = bu.shape[0]
    cb = SCAN_CB
    nt = t // tb

    def body(b_ref, a_ref, x_ref, carry):
        @pl.when(pl.program_id(1) == 0)
        def _():
            carry[...] = jnp.zeros_like(carry)

        ar, ai = a_ref[:, 0:cb], a_ref[:, cb:2 * cb]
        p1, p2, p4, p8, tr, ti = _scan_tables(ar, ai, False)

        def step(j, c):
            cr, ci = c
            i = pl.multiple_of(j * 8, 8)
            xr, xi = _tile_scan(b_ref[pl.ds(i, 8), 0:cb], b_ref[pl.ds(i, 8), cb:2 * cb], p1, p2, p4, False)
            mr, mi = _cmul(tr, ti, cr, ci)
            xr, xi = xr + mr, xi + mi
            x_ref[pl.ds(i, 8), 0:cb] = xr
            x_ref[pl.ds(i, 8), cb:2 * cb] = xi
            return xr[7:8, :], xi[7:8, :]

        cr, ci = lax.fori_loop(0, tb // 8, step, (carry[0:1, :], carry[1:2, :]), unroll=2)
        carry[0:1, :] = cr
        carry[1:2, :] = ci

    blk = pl.BlockSpec((tb, 2 * cb), lambda j, i: (i, j))
    return _call(
        body, name=name, grid=(SSM_CH // cb, nt),
        in_specs=[blk, pl.BlockSpec((1, 2 * cb), lambda j, i: (0, j))], out_specs=blk,
        out_shape=jax.ShapeDtypeStruct((t, 2 * SSM_CH), F32), scratch_shapes=[pltpu.VMEM((8, cb), F32)],
        sem=("parallel", "arbitrary"), args=(bu, a), comm=comm)


def _s5_scan_bwd(dx, x, a, name, tb=512, comm=None):
    t = dx.shape[0]
    cb = SCAN_CB
    nt = t // tb
    nj = tb // 8

    def body(d_ref, x_ref, xp_ref, a_ref, l_ref, da_ref, carry, acc):
        tblk = pl.program_id(1)

        @pl.when(tblk == 0)
        def _():
            carry[...] = jnp.zeros_like(carry)
            acc[...] = jnp.zeros_like(acc)

        ar, ai = a_ref[:, 0:cb], a_ref[:, cb:2 * cb]
        p1, p2, p4, p8, tr, ti = _scan_tables(ar, -ai, True)
        rows = lax.broadcasted_iota(jnp.int32, (8, cb), 0)

        def step(jj, c):
            cr, ci, sr_acc, si_acc = c
            j = nj - 1 - jj
            i = pl.multiple_of(j * 8, 8)
            lr, li = _tile_scan(d_ref[pl.ds(i, 8), 0:cb], d_ref[pl.ds(i, 8), cb:2 * cb], p1, p2, p4, True)
            mr, mi = _cmul(tr, ti, cr, ci)
            lr, li = lr + mr, li + mi
            l_ref[pl.ds(i, 8), 0:cb] = lr
            l_ref[pl.ds(i, 8), cb:2 * cb] = li
            ip = pl.multiple_of(jnp.maximum(j - 1, 0) * 8, 8)
            prev_r = jnp.where(j > 0, x_ref[pl.ds(ip, 8), 0:cb], xp_ref[:, 0:cb])
            prev_i = jnp.where(j > 0, x_ref[pl.ds(ip, 8), cb:2 * cb], xp_ref[:, cb:2 * cb])
            edge = jnp.where(jnp.logical_and(j == 0, tblk == nt - 1), 0.0, 1.0)
            xs_r = jnp.where(rows == 0, pltpu.roll(prev_r, 1, 0) * edge, pltpu.roll(x_ref[pl.ds(i, 8), 0:cb], 1, 0))
            xs_i = jnp.where(rows == 0, pltpu.roll(prev_i, 1, 0) * edge, pltpu.roll(x_ref[pl.ds(i, 8), cb:2 * cb], 1, 0))
            sr_acc = sr_acc + lr * xs_r + li * xs_i
            si_acc = si_acc + li * xs_r - lr * xs_i
            return lr[0:1, :], li[0:1, :], sr_acc, si_acc

        cr, ci, sr_acc, si_acc = lax.fori_loop(
            0, nj, step, (carry[0:1, :], carry[1:2, :], acc[:, 0:cb], acc[:, cb:2 * cb]))
        carry[0:1, :] = cr
        carry[1:2, :] = ci
        acc[:, 0:cb] = sr_acc
        acc[:, cb:2 * cb] = si_acc

        @pl.when(tblk == nt - 1)
        def _():
            da_ref[...] = jnp.sum(acc[...], axis=0, keepdims=True)

    blk = pl.BlockSpec((tb, 2 * cb), lambda j, i: (nt - 1 - i, j))
    prev = pl.BlockSpec((8, 2 * cb), lambda j, i: (jnp.maximum((nt - 1 - i) * (tb // 8) - 1, 0), j))
    vec = pl.BlockSpec((1, 2 * cb), lambda j, i: (0, j))
    return _call(
        body, name=name, grid=(SSM_CH // cb, nt), in_specs=[blk, blk, prev, vec], out_specs=[blk, vec],
        out_shape=[jax.ShapeDtypeStruct((t, 2 * SSM_CH), F32), jax.ShapeDtypeStruct((1, 2 * SSM_CH), F32)],
        scratch_shapes=[pltpu.VMEM((8, cb), F32), pltpu.VMEM((8, 2 * cb), F32)],
        sem=("parallel", "arbitrary"), args=(dx, x, x, a), comm=comm)


def _glu_fwd(yc, u, dvec, wg, bg, name, tr=512):
    t = yc.shape[0]

    def body(yc_ref, u_ref, d_ref, w_ref, b_ref, yl_ref, yb_ref):
        yl = yc_ref[...] + d_ref[...] * u_ref[...]
        yl_ref[...] = yl
        yg, _ = _gelu_and_grad(yl)
        z = jnp.dot(yg.astype(BF16), w_ref[...], preferred_element_type=F32) + b_ref[...]
        yb_ref[...] = (yg * _sigmoid(z)).astype(BF16)

    blk = pl.BlockSpec((tr, SSM_WIDTH), lambda i: (i, 0))
    vec = pl.BlockSpec((1, SSM_WIDTH), lambda i: (0, 0))
    return pl.pallas_call(
        body, name=name, grid=(t // tr,),
        in_specs=[blk, blk, vec, pl.BlockSpec((SSM_WIDTH, SSM_WIDTH), lambda i: (0, 0)), vec],
        out_specs=[blk, blk],
        out_shape=[jax.ShapeDtypeStruct((t, SSM_WIDTH), F32), jax.ShapeDtypeStruct((t, SSM_WIDTH), BF16)],
        compiler_params=_params(("parallel",)))(yc, u, dvec, wg, bg)


def _glu_bwd(yl, u, dvec, wg, bg, dyb, name, tr=512):
    t = yl.shape[0]

    def body(yl_ref, u_ref, d_ref, w_ref, b_ref, dy_ref, dyl_ref, du_ref, dw_ref, db_ref, dd_ref):
        @pl.when(pl.program_id(0) == 0)
        def _():
            dw_ref[...] = jnp.zeros_like(dw_ref)
            db_ref[...] = jnp.zeros_like(db_ref)
            dd_ref[...] = jnp.zeros_like(dd_ref)

        ylv, dyv, wv = yl_ref[...], dy_ref[...].astype(F32), w_ref[...]
        yg, dgelu = _gelu_and_grad(ylv)
        ygb = yg.astype(BF16)
        z = jnp.dot(ygb, wv, preferred_element_type=F32) + b_ref[...]
        sg = _sigmoid(z)
        dz = dyv * yg * sg * (1.0 - sg)
        dzb = dz.astype(BF16)
        dyg = dyv * sg + lax.dot_general(dzb, wv, (((1,), (1,)), ((), ())), preferred_element_type=F32)
        dyl = dyg * dgelu
        dyl_ref[...] = dyl.astype(BF16)
        du_ref[...] = dyl * d_ref[...]
        dw_ref[...] += lax.dot_general(ygb, dzb, (((0,), (0,)), ((), ())), preferred_element_type=F32)
        db_ref[...] += jnp.sum(dz, axis=0, keepdims=True)
        dd_ref[...] += jnp.sum(dyl * u_ref[...], axis=0, keepdims=True)

    blk = pl.BlockSpec((tr, SSM_WIDTH), lambda i: (i, 0))
    vec = pl.BlockSpec((1, SSM_WIDTH), lambda i: (0, 0))
    wsp = pl.BlockSpec((SSM_WIDTH, SSM_WIDTH), lambda i: (0, 0))
    return pl.pallas_call(
        body, name=name, grid=(t // tr,), in_specs=[blk, blk, vec, wsp, vec, blk],
        out_specs=[blk, blk, wsp, vec, vec],
        out_shape=[jax.ShapeDtypeStruct((t, SSM_WIDTH), BF16), jax.ShapeDtypeStruct((t, SSM_WIDTH), F32),
                   jax.ShapeDtypeStruct((SSM_WIDTH, SSM_WIDTH), F32), jax.ShapeDtypeStruct((1, SSM_WIDTH), F32),
                   jax.ShapeDtypeStruct((1, SSM_WIDTH), F32)],
        compiler_params=_params(("arbitrary",)))(yl, u, dvec, wg, bg, dyb)


def _mesh_pos():
    return lax.axis_index("x"), lax.axis_index("y"), lax.axis_index("c")


def _device_index():
    x, y, c = _mesh_pos()
    return 4 * x + 2 * y + c


def _gather_comm(arrays):
    na = len(arrays)

    def own_copy(ins, outs, sems, ai):
        return pltpu.make_async_copy(ins[ai], outs[ai].at[_device_index()], sems[2].at[ai])

    def ctx(ins, outs, sems):
        send_sems, recv_sems = sems[:2]
        x, y, c = _mesh_pos()
        chips = [(1 - x, y), (x, 1 - y), (1 - x, 1 - y)]

        def copy(ai, kk, block, to, own=False):
            slot = outs[ai].at[4 * block[0] + 2 * block[1] + block[2]]
            return pltpu.make_async_remote_copy(
                src_ref=ins[ai] if own else slot, dst_ref=slot, send_sem=send_sems.at[ai, kk],
                recv_sem=recv_sems.at[ai, kk], device_id=to, device_id_type=MESH)

        return (x, y, c), (x, y, 1 - c), chips, c, copy

    def start(ins, outs, sems):
        me, sibling, chips, c, copy = ctx(ins, outs, sems)
        for ai in range(na):
            copy(ai, 0, me, sibling, own=True).start()
            for j, chip in enumerate(chips):
                copy(ai, 1 + j, me, (*chip, c), own=True).start()
        for ai in range(na):
            own_copy(ins, outs, sems, ai).start()

    def mid(ins, outs, sems):
        me, sibling, chips, c, copy = ctx(ins, outs, sems)
        for ai in range(na):
            for j, chip in enumerate(chips):
                copy(ai, 1 + j, (*chip, c), me).wait_recv()
                copy(ai, 4 + j, (*chip, c), sibling).start()

    def end(ins, outs, sems):
        me, sibling, chips, c, copy = ctx(ins, outs, sems)
        for ai in range(na):
            copy(ai, 0, sibling, me).wait_recv()
            copy(ai, 0, me, sibling, own=True).wait_send()
            for j, chip in enumerate(chips):
                copy(ai, 4 + j, (*chip, 1 - c), me).wait_recv()
                copy(ai, 1 + j, me, (*chip, c), own=True).wait_send()
                copy(ai, 4 + j, (*chip, c), sibling).wait_send()
            own_copy(ins, outs, sems, ai).wait()

    return Comm(arrays, [jax.ShapeDtypeStruct((N_DEV,) + a.shape, a.dtype) for a in arrays],
                [pltpu.SemaphoreType.DMA((na, 7)), pltpu.SemaphoreType.DMA((na, 7)), pltpu.SemaphoreType.DMA((na,))],
                start, end, mid)


def _sequencer_gather(arrays, name, collective_id):
    comm = _gather_comm(arrays)
    na = len(arrays)

    def body(*refs):
        ins, outs, sems = refs[:na], refs[na:2 * na], refs[2 * na:]
        x, y, c = _mesh_pos()
        peers = [(x, y, 1 - c), (1 - x, y, c), (x, 1 - y, c), (1 - x, 1 - y, c)]
        barrier = pltpu.get_barrier_semaphore()
        for peer in peers:
            pl.semaphore_signal(barrier, inc=1, device_id=peer, device_id_type=MESH)
        pl.semaphore_wait(barrier, len(peers))
        comm.start(ins, outs, sems)
        comm.mid(ins, outs, sems)
        comm.end(ins, outs, sems)

    return list(pl.kernel(
        body, out_type=tuple(comm.out_shapes), mesh=plsc.ScalarSubcoreMesh(axis_name="sequencer", num_cores=1),
        name=name, scratch_types=tuple(comm.sems),
        compiler_params=pltpu.CompilerParams(collective_id=collective_id))(*arrays))


def _sequencer_exchange(comm, peers_of, name, collective_id):
    na = len(comm.inputs)

    def body(*refs):
        ins, outs, sems = refs[:na], refs[na:na + len(comm.out_shapes)], refs[na + len(comm.out_shapes):]
        peers = peers_of(*_mesh_pos())
        barrier = pltpu.get_barrier_semaphore()
        for peer in peers:
            pl.semaphore_signal(barrier, inc=1, device_id=peer, device_id_type=MESH)
        pl.semaphore_wait(barrier, len(peers))
        comm.start(ins, outs, sems)
        comm.end(ins, outs, sems)

    return list(pl.kernel(
        body, out_type=tuple(comm.out_shapes), mesh=plsc.ScalarSubcoreMesh(axis_name="sequencer", num_cores=1),
        name=name, scratch_types=tuple(comm.sems),
        compiler_params=pltpu.CompilerParams(collective_id=collective_id))(*comm.inputs))


SIBLING_SWAP_ID, CHIP_EXCHANGE_ID = 9, 10


def _sequencer_swap(arrays, name):
    return _sequencer_exchange(_swap_comm(arrays), lambda x, y, c: [(x, y, 1 - c)], name, SIBLING_SWAP_ID)[0]


def _sequencer_chips(send, name):
    return _sequencer_exchange(_chips_comm(send), lambda x, y, c: [(1 - x, y, c), (x, 1 - y, c), (1 - x, 1 - y, c)],
                               name, CHIP_EXCHANGE_ID)[0]


def _swap_comm(arrays):
    na = len(arrays)
    offs = np.concatenate([[0], np.cumsum([a.shape[1] for a in arrays])]).astype(int)

    def copies(ins, outs, sems):
        x, y, c = _mesh_pos()
        return [pltpu.make_async_remote_copy(
            src_ref=ins[ai].at[2 * k + 1 - c], dst_ref=outs[0].at[k, pl.ds(int(offs[ai]), arrays[ai].shape[1])],
            send_sem=sems[0].at[ai, k], recv_sem=sems[1].at[ai, k], device_id=(x, y, 1 - c), device_id_type=MESH)
            for ai in range(na) for k in range(4)]

    def start(ins, outs, sems):
        for cp in copies(ins, outs, sems):
            cp.start()

    def end(ins, outs, sems):
        for cp in copies(ins, outs, sems):
            cp.wait()

    return Comm(arrays, [jax.ShapeDtypeStruct((4, int(offs[-1]), PACK_COLS), arrays[0].dtype)],
                [pltpu.SemaphoreType.DMA((na, 4)), pltpu.SemaphoreType.DMA((na, 4))], start, end)


def _chips_comm(send):
    def copies(ins, outs, sems):
        x, y, c = _mesh_pos()
        chips = [(1 - x, y), (x, 1 - y), (1 - x, 1 - y)]
        return [pltpu.make_async_remote_copy(
            src_ref=ins[0].at[2 * cx + cy], dst_ref=outs[0].at[j], send_sem=sems[0].at[j], recv_sem=sems[1].at[j],
            device_id=(cx, cy, c), device_id_type=MESH) for j, (cx, cy) in enumerate(chips)]

    def start(ins, outs, sems):
        for cp in copies(ins, outs, sems):
            cp.start()

    def end(ins, outs, sems):
        for cp in copies(ins, outs, sems):
            cp.wait()

    return Comm([send], [jax.ShapeDtypeStruct((3,) + send.shape[1:], send.dtype)],
                [pltpu.SemaphoreType.DMA((3,)), pltpu.SemaphoreType.DMA((3,))], start, end)


def _pair_sum(keep, recv, name, tr=464):
    nchip, rows, cols = keep.shape

    def body(g_ref, r_ref, o_ref):
        o_ref[...] = (g_ref[...].astype(F32) + r_ref[...].astype(F32)).astype(BF16)

    blk = pl.BlockSpec((1, tr, cols), lambda k, i: (k, i, 0))
    return pl.pallas_call(
        body, name=name, grid=(nchip, rows // tr), in_specs=[blk, blk], out_specs=blk,
        out_shape=jax.ShapeDtypeStruct((nchip, rows, cols), BF16),
        compiler_params=_params(("parallel", "parallel")))(keep, recv)


def _pair_sum_pieces(pieces, recv, name, tr):
    _, rows, cols = pieces.shape
    core = lax.axis_index("c").astype(jnp.int32).reshape(1)

    def body(c_ref, g_ref, r_ref, o_ref):
        del c_ref
        o_ref[...] = (g_ref[...].astype(F32) + r_ref[...].astype(F32)).astype(BF16)

    grid_spec = pltpu.PrefetchScalarGridSpec(
        num_scalar_prefetch=1, grid=(4, rows // tr),
        in_specs=[pl.BlockSpec((1, tr, cols), lambda k, i, c_ref: (2 * k + c_ref[0], i, 0)),
                  pl.BlockSpec((1, tr, cols), lambda k, i, c_ref: (k, i, 0))],
        out_specs=pl.BlockSpec((1, tr, cols), lambda k, i, c_ref: (k, i, 0)))
    return pl.pallas_call(
        body, name=name, grid_spec=grid_spec, out_shape=jax.ShapeDtypeStruct((4, rows, cols), BF16),
        compiler_params=_params(("parallel", "parallel")))(core, pieces, recv)


def _chip_sum(own, others, name, tr=464):
    _, rows, cols = own.shape
    chip = (2 * lax.axis_index("x") + lax.axis_index("y")).astype(jnp.int32).reshape(1)

    def body(chip_ref, own_ref, oth_ref, o_ref):
        del chip_ref
        acc = own_ref[0].astype(F32)
        for j in range(3):
            acc = acc + oth_ref[j].astype(F32)
        o_ref[...] = acc

    grid_spec = pltpu.PrefetchScalarGridSpec(
        num_scalar_prefetch=1, grid=(rows // tr,),
        in_specs=[pl.BlockSpec((1, tr, cols), lambda i, chip_ref: (chip_ref[0], i, 0)),
                  pl.BlockSpec((3, tr, cols), lambda i, chip_ref: (0, i, 0))],
        out_specs=pl.BlockSpec((tr, cols), lambda i, chip_ref: (i, 0)))
    return pl.pallas_call(
        body, name=name, grid_spec=grid_spec, out_shape=jax.ShapeDtypeStruct((rows, cols), F32),
        compiler_params=_params(("parallel",)))(chip, own, others)


def _sum_leading(parts, name, tr=464):
    nparts, rows, cols = parts.shape
    tr = tr if rows % tr == 0 else rows

    def body(p_ref, o_ref):
        acc = p_ref[0].astype(F32)
        for i in range(1, nparts):
            acc = acc + p_ref[i].astype(F32)
        o_ref[...] = acc

    return pl.pallas_call(
        body, name=name, grid=(rows // tr,),
        in_specs=[pl.BlockSpec((nparts, tr, cols), lambda i: (0, i, 0))],
        out_specs=pl.BlockSpec((tr, cols), lambda i: (i, 0)), out_shape=jax.ShapeDtypeStruct((rows, cols), F32),
        compiler_params=_params(("parallel",)))(parts)


def _adamw(w, g, m, v, name, comm=None):
    shape = w.shape
    cols = shape[-1]
    lead = shape[0] if len(shape) >= 3 else 1
    rows = int(np.prod(shape[:-1])) // lead if len(shape) > 1 else 1
    w2, g2, m2, v2 = (a.reshape(lead, rows, cols) for a in (w, g, m, v))
    tr = max([t for t in range(8, min(rows, 512) + 1, 8) if rows % t == 0], default=rows)
    bc1, bc2 = 1.0 - ADAM_B1 ** ADAM_STEP, 1.0 - ADAM_B2 ** ADAM_STEP

    def body(w_ref, g_ref, m_ref, v_ref, d_ref, nm_ref, nv_ref):
        gv = g_ref[...]
        nm = ADAM_B1 * m_ref[...] + (1.0 - ADAM_B1) * gv
        nv = ADAM_B2 * v_ref[...] + (1.0 - ADAM_B2) * (gv * gv)
        nm_ref[...] = nm
        nv_ref[...] = nv
        d_ref[...] = -ADAM_LR * ((nm / bc1) / (jnp.sqrt(nv / bc2) + ADAM_EPS) + ADAM_WD * w_ref[...])

    blk = pl.BlockSpec((1, tr, cols), lambda l, i: (l, i, 0))
    res = _call(body, name=name, grid=(lead, rows // tr), in_specs=[blk] * 4, out_specs=[blk] * 3,
                out_shape=[jax.ShapeDtypeStruct((lead, rows, cols), F32)] * 3, sem=("parallel", "parallel"),
                args=(w2, g2, m2, v2), comm=comm)
    outs, couts = res if comm is not None else (res, None)
    outs = tuple(o.reshape(shape) for o in outs)
    return outs if comm is None else (outs, couts)


WEIGHT_NAMES = ['norm_mix_g', 'norm_xa_g', 'norm_ffn_g', 'norm_mem_g', 'norm_final_g', 'w_in_ab', 'conv_qkv_a',
                'a_log_a', 'dt_bias_a', 'onorm_g_a', 'ssm_lambda_re', 'ssm_lambda_im', 'ssm_b_re', 'ssm_b_im',
                'ssm_c_re', 'ssm_c_im', 'ssm_d', 'ssm_log_dt', 'w_glu_b', 'b_glu_b', 'w_out_ab', 'pool_w',
                'pool_scale', 'xa_wq', 'xa_wkv', 'xa_wo', 'ffn_w_up', 'ffn_conv', 'ffn_w_down']
BIG_SHARDED = {'w_in_ab': ((1, 1024, 2568), 2), 'w_glu_b': ((1, 512, 512), 1), 'w_out_ab': ((1, 1024, 1024), 1),
               'pool_w': ((1, 4, 256, 256), 2), 'xa_wq': ((2, 1024, 1024), 1), 'xa_wkv': ((2, 1024, 2048), 2),
               'xa_wo': ((2, 1024, 1024), 1), 'ffn_w_up': ((2, 1024, 5632), 2), 'ffn_w_down': ((2, 2816, 1024), 1)}
SMALL_SHARDED = {'conv_qkv_a': ((1, 4, 1536), 2), 'pool_scale': ((1, 1024), 1), 'ffn_conv': ((2, 3, 5632), 2)}
REPLICATED = {'norm_mix_g': (2, 1024), 'norm_xa_g': (2, 1024), 'norm_ffn_g': (2, 1024), 'norm_mem_g': (1024,),
              'norm_final_g': (1024,), 'a_log_a': (1, 4), 'dt_bias_a': (1, 4), 'onorm_g_a': (1, 128),
              'ssm_lambda_re': (1, 32, 64), 'ssm_lambda_im': (1, 32, 64), 'ssm_b_re': (1, 32, 64, 16),
              'ssm_b_im': (1, 32, 64, 16), 'ssm_c_re': (1, 32, 16, 64), 'ssm_c_im': (1, 32, 16, 64),
              'ssm_d': (1, 32, 16), 'ssm_log_dt': (1, 32), 'b_glu_b': (1, 512)}
PACK_ROW_ALIGN = 8


def _shard_shape(shape, axis):
    return tuple(s // N_DEV if i == axis else s for i, s in enumerate(shape))


def _round_up(n, m):
    return (n + m - 1) // m * m


def _pack(arrays):
    total = sum(int(np.prod(a.shape)) for a in arrays)
    padded = _round_up(total, PACK_COLS * PACK_ROW_ALIGN)
    parts = [a.astype(F32).reshape(-1) for a in arrays]
    if padded != total:
        parts.append(jnp.zeros((padded - total,), F32))
    return jnp.concatenate(parts).reshape(padded // PACK_COLS, PACK_COLS)


def _unpack(packed, shapes):
    flat, out, off = packed.reshape(-1), [], 0
    for shape in shapes:
        size = int(np.prod(shape))
        out.append(flat[off:off + size].reshape(shape))
        off += size
    return out


def _split_shards(full, axis):
    shape = full.shape
    s = shape[axis] // N_DEV
    a = full.reshape(shape[:axis] + (N_DEV, s) + shape[axis + 1:])
    return jnp.moveaxis(a, axis, 0).reshape(N_DEV, -1)


def _merge_shards(pieces, shape, axis):
    sh = _shard_shape(shape, axis)
    a = pieces.reshape((N_DEV,) + sh)
    a = jnp.moveaxis(a, 0, axis)
    return a.reshape(shape)


_SCAN_NB = SSM_CH // SCAN_CB


def _to_scan_layout(m, axis):
    shape = m.shape
    m = m.reshape(shape[:axis] + (2, _SCAN_NB, SCAN_CB) + shape[axis + 1:])
    return jnp.swapaxes(m, axis, axis + 1).reshape(shape)


def _from_scan_layout(m, axis):
    shape = m.shape
    m = m.reshape(shape[:axis] + (_SCAN_NB, 2, SCAN_CB) + shape[axis + 1:])
    return jnp.swapaxes(m, axis, axis + 1).reshape(shape)


def _s5_discretise(lam_re, lam_im, b_re, b_im, log_dt):
    dt = jnp.exp(log_dt)[:, None]
    mag = jnp.exp(lam_re * dt)
    ang = lam_im * dt
    lb_re, lb_im = mag * jnp.cos(ang), mag * jnp.sin(ang)
    den = lam_re * lam_re + lam_im * lam_im
    nr, ni = lb_re - 1.0, lb_im
    coef_re = (nr * lam_re + ni * lam_im) / den
    coef_im = (ni * lam_re - nr * lam_im) / den
    bb_re = coef_re[..., None] * b_re - coef_im[..., None] * b_im
    bb_im = coef_re[..., None] * b_im + coef_im[..., None] * b_re
    return lb_re, lb_im, bb_re, bb_im


_GROUPS_PER_BLOCK = N_GROUPS // _SCAN_NB
_U_BLOCK = _GROUPS_PER_BLOCK * SSM_GROUP


def _s5_matrices(lb_re, lb_im, bb_re, bb_im, c_re, c_im):
    eye = jnp.eye(_GROUPS_PER_BLOCK, dtype=F32)
    blocked = lambda m: m.reshape((_SCAN_NB, _GROUPS_PER_BLOCK) + m.shape[1:])
    bmat = lambda bb: jnp.einsum('jgph,gk->jghkp', blocked(bb), eye).reshape(_SCAN_NB, _U_BLOCK, SCAN_CB)
    cmat = lambda cc: jnp.einsum('jghp,gk->jkpgh', blocked(cc), eye).reshape(_SCAN_NB, SCAN_CB, _U_BLOCK)
    b_in = jnp.concatenate([bmat(bb_re), bmat(bb_im)], axis=2)
    c_out = jnp.concatenate([cmat(c_re), -cmat(c_im)], axis=1)
    a_row = _to_scan_layout(jnp.concatenate([lb_re.reshape(1, SSM_CH), lb_im.reshape(1, SSM_CH)], axis=1), 1)
    return b_in, c_out, a_row


def _s5_matrix_grads(db_in, dc_out, da_row):
    da_nat = _from_scan_layout(da_row, 1)
    eye = jnp.eye(_GROUPS_PER_BLOCK, dtype=F32)
    nb, gb = _SCAN_NB, _GROUPS_PER_BLOCK
    bgrad = lambda m: jnp.einsum('jghkp,gk->jgph', m.reshape(nb, gb, SSM_GROUP, gb, SSM_STATE), eye
                                 ).reshape(N_GROUPS, SSM_STATE, SSM_GROUP)
    cgrad = lambda m: jnp.einsum('jkpgh,gk->jghp', m.reshape(nb, gb, SSM_STATE, gb, SSM_GROUP), eye
                                 ).reshape(N_GROUPS, SSM_GROUP, SSM_STATE)
    dbb_re, dbb_im = bgrad(db_in[:, :, :SCAN_CB]), bgrad(db_in[:, :, SCAN_CB:])
    dc_re, dc_im = cgrad(dc_out[:, :SCAN_CB]), -cgrad(dc_out[:, SCAN_CB:])
    dlb_re = da_nat[0, :SSM_CH].reshape(N_GROUPS, SSM_STATE)
    dlb_im = da_nat[0, SSM_CH:].reshape(N_GROUPS, SSM_STATE)
    return dlb_re, dlb_im, dbb_re, dbb_im, dc_re, dc_im


def _as_pieces(a):
    return a.reshape(N_DEV, a.shape[0] // N_DEV, a.shape[1])


def _hybrid_fwd(xn, x, wts, p, weights, riders):
    sv = {}
    hq = _mm(xn, wts['w_qkv_t'], "nt", "l0_in_qkv")
    gate = _mm(xn, wts['w_gate_t'], "nt", "l0_in_gate")
    ba = _mm(xn, wts['w_ba_t'], "nt", "l0_in_ba")
    u = _mm(xn, wts['w_u_t'], "nt", "l0_in_u")
    conv = p['conv_qkv']
    qkv = _qkv_pre_fwd(hq, conv, "l0_qkv_pre")
    gates = _gates_fwd(ba, p['arow'], p['brow'], "l0_gates")
    o, tm_all, s_all = riders.run("l0_gdr_fwd", _gdr_fwd, qkv, gates)
    wts['w_glu'], wts['w_out'] = weights.full['w_glu'], weights.full['w_out']
    y_a = _onorm_fwd(o, gate, p['onorm_g'], "l0_onorm")
    bu = riders.run("l0_s5_bu", _mm_bd, u, p['b_in'], "nn")
    xs = riders.run("l0_s5_scan", _s5_scan_fwd, bu, p['a_row'])
    weights.gather_by_sequencer(GATHER_LAYER1, xs, "gather_layer1", GATHER_LAYER1_ID)
    yc = riders.run("l0_s5_cx", _mm_bd, xs, p['c_out'], "nn")
    yl, y_b = _glu_fwd(yc, u, p['d_row'], wts['w_glu'], p['b_glu'], "l0_glu")
    mixed = jnp.concatenate([y_a, y_b], axis=1)
    x1 = _mm(mixed, wts['w_out'], "nn", "l0_out", res=x)
    sv.update(hq=hq, gate=gate, ba=ba, u=u, qkv=qkv, gb=gates, o=o, tm=tm_all, s=s_all, xs=xs, yl=yl, mixed=mixed)
    return x1, sv


def _hybrid_bwd(dx1, xn, wts, p, sv, riders):
    gr = {}
    dmixed = _mm(dx1, wts['w_out'], "nt", "l0_out_dx", out_dtype=BF16)
    riders.grad('w_out', _as_pieces(_mm(sv['mixed'], dx1, "tn", "l0_out_dw", out_dtype=BF16)))
    dya, dyb = dmixed[:, :WIDTH_A], dmixed[:, WIDTH_A:]
    dyl, du_direct, dw_glu, gr['b_glu_b'], dd = _glu_bwd(
        sv['yl'], sv['u'], p['d_row'], wts['w_glu'], p['b_glu'], dyb, "l0_glu_bwd")
    riders.grad('w_glu', dw_glu.astype(BF16).reshape(N_DEV, -1, PACK_COLS))
    dxs = riders.run("l0_s5_cx_dx", _mm_bd, dyl, p['c_out'], "nt")
    dc_out = _mm_bd(sv['xs'], dyl, "tn", "l0_s5_cx_dw")
    lam, da_row = riders.run("l0_s5_scan_bwd", _s5_scan_bwd, dxs, sv['xs'], p['a_row'])
    du = _mm_bd(lam, p['b_in'], "nt", "l0_s5_bu_dx", res=du_direct, out_dtype=BF16)
    db_in = _mm_bd(sv['u'], lam, "tn", "l0_s5_bu_dw")
    gr['s5'] = (db_in, dc_out, da_row, dd)
    do, dgate, gr['onorm_g_a'] = _onorm_bwd(sv['o'], sv['gate'], p['onorm_g'], dya, "l0_onorm_bwd")
    dqkv, dgb = riders.run("l0_gdr_bwd", _gdr_bwd, sv['qkv'], sv['gb'], sv['tm'], sv['s'], do)
    dhq, gr['conv_qkv_a'] = _qkv_pre_bwd(sv['hq'], p['conv_qkv'], dqkv, "l0_qkv_pre_bwd")
    dba, da_log, ddt_bias = _gates_bwd(sv['ba'], p['arow'], p['brow'], dgb, "l0_gates_bwd")
    gr['a_log_a'], gr['dt_bias_a'] = da_log[:, 4:8], ddt_bias[:, 4:8]
    dw_qkv_t = _mm(dhq, xn, "tn", "l0_in_qkv_dw", out_dtype=BF16)
    dw_gate_t = _mm(dgate, xn, "tn", "l0_in_gate_dw", out_dtype=BF16)
    dw_ba_t = _mm(dba, xn, "tn", "l0_in_ba_dw", out_dtype=BF16)
    dw_u_t = _mm(du, xn, "tn", "l0_in_u_dw", out_dtype=BF16)
    dw_in_t = _as_pieces(jnp.concatenate([dw_qkv_t, dw_gate_t, dw_ba_t[:8], dw_u_t], axis=0))
    riders.grad('w_in_t', jnp.concatenate(
        [dw_in_t, jnp.zeros((N_DEV, dict(PIECES)['w_in_t'] - W_IN_PIECE, D_MODEL), BF16)], axis=1))
    dxn = riders.run("l0_in_qkv_dx", _mm, dhq, wts['w_qkv_t'], "nn")
    dxn = _mm(dgate, wts['w_gate_t'], "nn", "l0_in_gate_dx", res=dxn)
    dxn = _mm(dba, wts['w_ba_t'], "nn", "l0_in_ba_dx", res=dxn)
    dxn = riders.run("l0_in_u_dx", _mm, du, wts['w_u_t'], "nn", res=dxn)
    return dxn, gr


def _xa_fwd(x1, g, mem_n, wq, wkv_t, wo, tag, riders):
    xq = _rms_fwd(x1, g, BF16, tag + "_norm")
    q = _mm(xq, wq, "nn", tag + "_q", out_dtype=BF16)
    kv = _mm(mem_n, wkv_t, "nt", tag + "_kv", out_dtype=BF16)
    o = riders.run(tag + "_attn", _attn_fwd, q, kv)
    x2 = _mm(o, wo, "nn", tag + "_o", res=x1)
    return x2, dict(xq=xq, q=q, kv=kv, o=o)


def _xa_bwd(dx2, x1, g, mem_n, wq, wkv_t, wo, sv, tag, layer, riders):
    do = _mm(dx2, wo, "nt", tag + "_o_dx", out_dtype=BF16)
    riders.grad('wo%d' % layer, _as_pieces(_mm(sv['o'], dx2, "tn", tag + "_o_dw", out_dtype=BF16)))
    dq, dk, dv = _attn_bwd(sv['q'], sv['kv'], do, tag + "_attn_bwd")
    dkv = jnp.concatenate([dk, dv], axis=1).astype(BF16)
    dxq = _mm(dq, wq, "nt", tag + "_q_dx")
    riders.grad('wq%d' % layer, _as_pieces(_mm(sv['xq'], dq, "tn", tag + "_q_dw", out_dtype=BF16)))
    dmem_n = _mm(dkv, wkv_t, "nn", tag + "_kv_dx")
    riders.grad('wkv_t%d' % layer, _as_pieces(_mm(dkv, mem_n, "tn", tag + "_kv_dw", out_dtype=BF16)))
    dx1, dg = riders.run(tag + "_norm_bwd", _rms_bwd, x1, g, dxq, dx2)
    return dx1, dmem_n, dg


def _ffn_fwd(x2, g, w_up_t, conv, w_down, tag, riders):
    xf = _rms_fwd(x2, g, BF16, tag + "_norm")
    h = riders.run(tag + "_up", _mm, xf, w_up_t, "nt")
    a = riders.run(tag + "_act", _ffn_act_fwd, h, conv)
    x3 = _mm(a, w_down, "nn", tag + "_down", res=x2)
    return x3, dict(xf=xf, h=h, a=a)


def _ffn_bwd(dx3, x2, g, w_up_t, conv, w_down, sv, tag, layer, riders):
    da = _mm(dx3, w_down, "nt", tag + "_down_dx")
    riders.grad('down%d' % layer, _as_pieces(_mm(sv['a'], dx3, "tn", tag + "_down_dw", out_dtype=BF16)))
    dh, dconv = riders.run(tag + "_act_bwd", _ffn_act_bwd, sv['h'], conv, da)
    dxf = riders.run(tag + "_up_dx", _mm_parts, dh, w_up_t, "nn")
    dw_up_t = riders.run(tag + "_up_dw", _mm_parts, dh, sv['xf'], "tn", out_dtype=BF16)
    riders.grad('up_t%d' % layer, _as_pieces(dw_up_t))
    dx2, dg = riders.run(tag + "_norm_bwd", _rms_bwd, x2, g, dxf, dx3)
    return dx2, dconv, dg


BIG_NAMES, SMALL_NAMES, REP_NAMES = list(BIG_SHARDED), list(SMALL_SHARDED), list(REPLICATED)
SMALL_SIZES = [int(np.prod(_shard_shape(*SMALL_SHARDED[n]))) for n in SMALL_NAMES]


PIECES = [('w_in_t', 384), ('w_glu', 32), ('w_out', 128), ('pool_w', 32), ('wq0', 128), ('wq1', 128),
          ('wkv_t0', 256), ('wkv_t1', 256), ('wo0', 128), ('wo1', 128), ('up_t0', 704), ('up_t1', 704),
          ('down0', 352), ('down1', 352)]
W_IN_ROWS = 4 * WIDTH_A + 2 * N_HEADS_A + SSM_WIDTH
W_IN_PIECE = W_IN_ROWS // N_DEV


def _row_tile(rows):
    return max(t for t in range(16, min(rows, 512) + 1, 16) if rows % t == 0)


class _Riders:
    def __init__(self):
        self.waiting = {}
        self.deferred = {}
        self.grads = {}
        self.groups = []
        self.reduced = {}

    def add(self, host, comm, then):
        self.waiting.setdefault(host, []).append((comm, then))

    def after(self, marker, then):
        self.deferred.setdefault(marker, []).append(then)

    def mark(self, name, out=None):
        for cont in self.deferred.pop(name, []):
            step = cont()
            if step is not None:
                values, then = step
                out, values = lax.optimization_barrier((out, values))
                then(values)
        return out

    def run(self, name, fn, *args, **kw):
        riders = self.waiting.pop(name, [])
        if not riders:
            out = fn(*args, name=name, **kw)
        else:
            out, couts = fn(*args, name=name, comm=[c for c, _ in riders], **kw)
            for (_, then), got in zip(riders, couts):
                then(got)
        return self.mark(name, out)

    def grad(self, key, pieces):
        self.grads[key] = pieces
        for group in [g for g in self.groups if all(k in self.grads for k in g[1])]:
            self.groups.remove(group)
            self._reduce(*group)

    def _reduce(self, name, keys, pair_marker, sum_marker):
        arrays = [self.grads[k] for k in keys]
        rows = sum(a.shape[1] for a in arrays)
        tile = _row_tile(rows)
        from_sibling = _sequencer_swap(arrays, name + "_to_sibling")

        def after_swap():
            if len(arrays) == 1:
                chip_sums = _pair_sum_pieces(arrays[0], from_sibling, name + "_pair_sum", tr=tile)
            else:
                core = lax.axis_index("c")
                keep = jnp.concatenate(
                    [lax.dynamic_index_in_dim(a.reshape(4, 2, a.shape[1], PACK_COLS), core, 1, keepdims=False)
                     for a in arrays], axis=1)
                chip_sums = _pair_sum(keep, from_sibling, name + "_pair_sum", tr=tile)

            def exchange_among_chips(chip_sums):
                from_chips = _sequencer_chips(chip_sums, name + "_to_chips")

                def store(total):
                    off = 0
                    for k, a in zip(keys, arrays):
                        self.reduced[k] = total[off:off + a.shape[1]]
                        off += a.shape[1]

                self.after(sum_marker, lambda: (
                    _chip_sum(chip_sums, from_chips, name + "_chip_sum", tr=tile), store))

            return chip_sums, exchange_among_chips

        self.after(pair_marker, after_swap)


class _Weights:
    def __init__(self, inp):
        bf = lambda a: a.astype(BF16)
        local = {'w_in_t': bf(inp['w_in_ab'][0]).T, 'w_glu': bf(inp['w_glu_b'][0]), 'w_out': bf(inp['w_out_ab'][0]),
                 'pool_w': bf(inp['pool_w'][0]),
                 'small': _pack([inp[n] for n in SMALL_NAMES])}
        for l in range(2):
            local['wq%d' % l] = bf(inp['xa_wq'][l])
            local['wkv_t%d' % l] = bf(inp['xa_wkv'][l]).T
            local['wo%d' % l] = bf(inp['xa_wo'][l])
            local['up_t%d' % l] = bf(inp['ffn_w_up'][l]).T
            local['down%d' % l] = bf(inp['ffn_w_down'][l])
        self.local, self.full = local, {}

    def plan(self, keys):
        return _gather_comm([self.local[k] for k in keys])

    def gather_by_sequencer(self, keys, after, name, collective_id):
        arrays = [self.local[k] for k in keys]
        tie = (after.reshape(-1)[0] * 0.0).astype(arrays[0].dtype)
        arrays[0] = arrays[0] + tie
        self.land(keys, _sequencer_gather(arrays, name, collective_id))

    def land(self, keys, gathered):
        for k, g in zip(keys, gathered):
            if k == 'small':
                off = 0
                for n, size in zip(SMALL_NAMES, SMALL_SIZES):
                    self.full[n] = _merge_shards(g.reshape(N_DEV, -1)[:, off:off + size], *SMALL_SHARDED[n])
                    off += size
            elif k == 'pool_w':
                self.full[k] = jnp.swapaxes(g, 0, 1).reshape(len(POOL_WINDOWS), POOL_GROUP, POOL_GROUP)
            else:
                self.full[k] = g.reshape(N_DEV * g.shape[1], g.shape[2])


GATHER_FIRST = ['w_in_t', 'small']
GATHER_LAYER0 = ['w_glu', 'w_out', 'wq0', 'wkv_t0', 'wo0', 'down0', 'up_t0']
GATHER_LAYER1 = ['pool_w', 'wq1', 'wkv_t1', 'wo1', 'up_t1', 'down1']
GATHER_LAYER0_ID, GATHER_LAYER1_ID = 7, 8
GRAD_RIDES = [('g_down1', ['down1'], 'l1_ffn_up_dx', 'l1_xa_norm_bwd'),
              ('g_up1', ['up_t1'], 'l1_xa_norm_bwd', 'l0_ffn_up_dx'),
              ('g_xa1', ['wq1', 'wkv_t1', 'wo1', 'pool_w'], 'l0_ffn_up_dx', 'l0_xa_norm_bwd'),
              ('g_down0', ['down0'], 'l0_ffn_up_dx', 'l0_s5_scan_bwd'),
              ('g_l0', ['up_t0', 'wq0', 'wkv_t0', 'wo0'], 'l0_s5_cx_dx', 'l0_in_u_dx'),
              ('g_out', ['w_out', 'w_glu'], 'l0_s5_scan_bwd', 'l0_in_u_dx'),
              ('g_in', ['w_in_t'], 'l0_in_u_dx', 'adamw_pool_w')]


def _local_step(inp):
    f32_of = lambda n: inp[n].astype(F32)
    weights = _Weights(inp)
    riders = _Riders()
    riders.groups = list(GRAD_RIDES)
    full = weights.full
    weights.land(GATHER_FIRST, _comm_only(weights.plan(GATHER_FIRST), "gather_first"))
    weights.gather_by_sequencer(GATHER_LAYER0, full['w_in_t'], "gather_layer0", GATHER_LAYER0_ID)
    w_in_t = full['w_in_t']
    wts0 = dict(w_qkv_t=w_in_t[:3 * WIDTH_A], w_gate_t=w_in_t[3 * WIDTH_A:4 * WIDTH_A],
                w_ba_t=jnp.concatenate([w_in_t[4 * WIDTH_A:4 * WIDTH_A + 8], jnp.zeros((LANE - 8, D_MODEL), BF16)], 0),
                w_u_t=w_in_t[4 * WIDTH_A + 8:])
    lb_disc, disc_vjp = jax.vjp(_s5_discretise, f32_of('ssm_lambda_re')[0], f32_of('ssm_lambda_im')[0],
                                f32_of('ssm_b_re')[0], f32_of('ssm_b_im')[0], f32_of('ssm_log_dt')[0])
    b_in, c_out, a_row = _s5_matrices(*lb_disc, f32_of('ssm_c_re')[0], f32_of('ssm_c_im')[0])
    zeros4 = jnp.zeros((1, 4), F32)
    p0 = dict(conv_qkv=full['conv_qkv_a'][0], onorm_g=f32_of('onorm_g_a'),
              arow=jnp.concatenate([zeros4, f32_of('a_log_a'), jnp.zeros((1, LANE - 8), F32)], 1),
              brow=jnp.concatenate([zeros4, f32_of('dt_bias_a'), jnp.zeros((1, LANE - 8), F32)], 1),
              b_in=b_in.astype(BF16), c_out=c_out.astype(BF16), a_row=a_row,
              d_row=f32_of('ssm_d').reshape(1, SSM_WIDTH), b_glu=f32_of('b_glu_b'))

    x0 = inp['x'][0]
    mem_n = _rms_fwd(inp['mem'][0], inp['norm_mem_g'], BF16, "mem_norm")
    xn0 = _rms_fwd(x0, inp['norm_mix_g'][0], BF16, "l0_mix_norm")
    x1, sv_mix0 = _hybrid_fwd(xn0, x0, wts0, p0, weights, riders)
    x2, sv_xa0 = _xa_fwd(x1, inp['norm_xa_g'][0], mem_n, full['wq0'], full['wkv_t0'], full['wo0'], "l0_xa", riders)
    x3, sv_ffn0 = _ffn_fwd(x2, inp['norm_ffn_g'][0], full['up_t0'], full['ffn_conv'][0], full['down0'], "l0_ffn", riders)
    xn1 = _rms_fwd(x3, inp['norm_mix_g'][1], F32, "l1_mix_norm")
    x4 = _pool_fwd(xn1, full['pool_w'], full['pool_scale'], x3, "l1_pool")
    x5, sv_xa1 = _xa_fwd(x4, inp['norm_xa_g'][1], mem_n, full['wq1'], full['wkv_t1'], full['wo1'], "l1_xa", riders)
    x6, sv_ffn1 = _ffn_fwd(x5, inp['norm_ffn_g'][1], full['up_t1'], full['ffn_conv'][1], full['down1'], "l1_ffn", riders)
    loss_part, dx6, dg_final = _loss_head(x6, inp['norm_final_g'], inp['loss_target'][0], "loss_head")

    dx5, dconv1, dg_ffn1 = _ffn_bwd(dx6, x5, inp['norm_ffn_g'][1], full['up_t1'], full['ffn_conv'][1], full['down1'],
                                    sv_ffn1, "l1_ffn", 1, riders)
    dx4, dmem1, dg_xa1 = _xa_bwd(dx5, x4, inp['norm_xa_g'][1], mem_n, full['wq1'], full['wkv_t1'], full['wo1'],
                                 sv_xa1, "l1_xa", 1, riders)
    dxn1, dpool_w, dpool_scale = _pool_bwd(xn1, full['pool_w'], full['pool_scale'], dx4, "l1_pool_bwd")
    pool_pieces = jnp.swapaxes(dpool_w.astype(BF16).reshape(len(POOL_WINDOWS), N_DEV, -1, POOL_GROUP), 0, 1)
    riders.grad('pool_w', pool_pieces.reshape(N_DEV, -1, PACK_COLS))
    dx3, dg_mix1 = riders.run("l1_mix_norm_bwd", _rms_bwd, x3, inp['norm_mix_g'][1], dxn1, dx4)
    dx2, dconv0, dg_ffn0 = _ffn_bwd(dx3, x2, inp['norm_ffn_g'][0], full['up_t0'], full['ffn_conv'][0], full['down0'],
                                    sv_ffn0, "l0_ffn", 0, riders)
    dx1, dmem0, dg_xa0 = _xa_bwd(dx2, x1, inp['norm_xa_g'][0], mem_n, full['wq0'], full['wkv_t0'], full['wo0'],
                                 sv_xa0, "l0_xa", 0, riders)
    dxn0, g_mix0 = _hybrid_bwd(dx1, xn0, wts0, p0, sv_mix0, riders)
    grad_x, dg_mix0 = _rms_bwd(x0, inp['norm_mix_g'][0], dxn0, dx1, "l0_mix_norm_bwd")
    _, dg_mem = _rms_bwd(inp['mem'][0], inp['norm_mem_g'], dmem0 + dmem1, None, "mem_norm_bwd")
    assert not riders.groups and not riders.waiting and all(k.startswith("adamw_") for k in riders.deferred), (
        riders.groups, list(riders.waiting), list(riders.deferred))

    db_in, dc_out, da_row, dd = g_mix0['s5']
    dlb_re, dlb_im, dbb_re, dbb_im, dc_re, dc_im = _s5_matrix_grads(db_in, dc_out, da_row)
    dlam_re, dlam_im, dbr, dbi, dlog_dt = disc_vjp((dlb_re, dlb_im, dbb_re, dbb_im))

    rep_grads = {
        'norm_mix_g': jnp.concatenate([dg_mix0, dg_mix1], 0), 'norm_xa_g': jnp.concatenate([dg_xa0, dg_xa1], 0),
        'norm_ffn_g': jnp.concatenate([dg_ffn0, dg_ffn1], 0), 'norm_mem_g': dg_mem.reshape(-1),
        'norm_final_g': dg_final.reshape(-1), 'a_log_a': g_mix0['a_log_a'], 'dt_bias_a': g_mix0['dt_bias_a'],
        'onorm_g_a': g_mix0['onorm_g_a'], 'ssm_lambda_re': dlam_re[None], 'ssm_lambda_im': dlam_im[None],
        'ssm_b_re': dbr[None], 'ssm_b_im': dbi[None], 'ssm_c_re': dc_re[None], 'ssm_c_im': dc_im[None],
        'ssm_d': dd.reshape(1, N_GROUPS, SSM_GROUP), 'ssm_log_dt': dlog_dt[None], 'b_glu_b': g_mix0['b_glu_b']}
    small_grads = {'conv_qkv_a': g_mix0['conv_qkv_a'][None], 'pool_scale': dpool_scale,
                   'ffn_conv': jnp.stack([dconv0, dconv1])}
    return loss_part, grad_x, riders, rep_grads, small_grads


ADAMW_ORDER = ['ffn_w_up', 'ffn_w_down', 'xa_wkv', 'xa_wq', 'xa_wo', 'w_out_ab', 'w_glu_b', 'pool_w', 'w_in_ab']


def _update(inp, loss_part, grad_x, riders, rep_grads, small_grads):
    dev = _device_index()
    misc_local = _pack([rep_grads[n] for n in REP_NAMES] + [small_grads[n] for n in SMALL_NAMES] + [loss_part])
    (misc_all,) = _sequencer_gather([misc_local], "gather_small_grads", GATHER_LAYER0_ID)
    piece = lambda key: riders.reduced[key]
    both = lambda name: jnp.stack([piece(name + '0'), piece(name + '1')])
    swap = lambda a: jnp.swapaxes(a, -1, -2)
    reduced = {'w_in_ab': lambda: piece('w_in_t')[:W_IN_PIECE][None],
               'w_glu_b': lambda: piece('w_glu').reshape(inp['w_glu_b'].shape),
               'w_out_ab': lambda: piece('w_out')[None], 'pool_w': lambda: piece('pool_w').reshape(inp['pool_w'].shape),
               'xa_wq': lambda: both('wq'), 'xa_wkv': lambda: both('wkv_t'), 'xa_wo': lambda: both('wo'),
               'ffn_w_up': lambda: both('up_t'), 'ffn_w_down': lambda: both('down')}
    transposed = ('w_in_ab', 'xa_wkv', 'ffn_w_up')
    grads, upd = {}, {}
    assert sorted(ADAMW_ORDER) == sorted(BIG_NAMES)
    for n in ADAMW_ORDER:
        fix = swap if n in transposed else (lambda a: a)
        g = reduced[n]()
        out = riders.run("adamw_" + n, _adamw, fix(inp[n]), g, fix(inp['m_' + n]), fix(inp['v_' + n]))
        upd[n], grads[n] = tuple(fix(o) for o in out), fix(g)
    assert not riders.waiting and not riders.deferred, (list(riders.waiting), list(riders.deferred))
    misc_sum = _sum_leading(misc_all, "small_grads_sum")
    misc = _unpack(misc_sum, [inp[n].shape for n in REP_NAMES] + [SMALL_SHARDED[n][0] for n in SMALL_NAMES] + [()])
    loss = misc.pop()
    for n, g in zip(REP_NAMES, misc):
        grads[n] = g
    for n, g in zip(SMALL_NAMES, misc[len(REP_NAMES):]):
        grads[n] = lax.dynamic_index_in_dim(_split_shards(g, SMALL_SHARDED[n][1]), dev, 0, keepdims=False
                                            ).reshape(inp[n].shape)
    tiny_names = REP_NAMES + SMALL_NAMES
    rep_total = sum(int(np.prod(inp[n].shape)) for n in REP_NAMES)
    packs = [_pack([inp[prefix + n] for n in tiny_names]) for prefix in ('', 'm_', 'v_')]
    g_pack = _pack([misc_sum.reshape(-1)[:rep_total]] + [grads[n] for n in SMALL_NAMES])
    tiny_out = [_unpack(o, [inp[n].shape for n in tiny_names])
                for o in _adamw(packs[0], g_pack, packs[1], packs[2], "adamw_small")]
    for i, n in enumerate(tiny_names):
        upd[n] = tuple(o[i] for o in tiny_out)

    outs = [loss, grad_x[None]]
    outs += [grads[n] for n in WEIGHT_NAMES]
    for i in range(3):
        outs += [upd[n][i] for n in WEIGHT_NAMES]
    return tuple(outs)


def _step(inp):
    loss_part, grad_x, riders, rep_grads, small_grads = _local_step(inp)
    return _update(inp, loss_part, grad_x, riders, rep_grads, small_grads)


INPUT_NAMES = (['x', 'mem'] + WEIGHT_NAMES + ['loss_target'] + ['m_' + n for n in WEIGHT_NAMES]
               + ['v_' + n for n in WEIGHT_NAMES])


def kernel(x, mem, norm_mix_g, norm_xa_g, norm_ffn_g, norm_mem_g, norm_final_g, w_in_ab, conv_qkv_a, a_log_a, dt_bias_a, onorm_g_a, ssm_lambda_re, ssm_lambda_im, ssm_b_re, ssm_b_im, ssm_c_re, ssm_c_im, ssm_d, ssm_log_dt, w_glu_b, b_glu_b, w_out_ab, pool_w, pool_scale, xa_wq, xa_wkv, xa_wo, ffn_w_up, ffn_conv, ffn_w_down, loss_target, m_norm_mix_g, m_norm_xa_g, m_norm_ffn_g, m_norm_mem_g, m_norm_final_g, m_w_in_ab, m_conv_qkv_a, m_a_log_a, m_dt_bias_a, m_onorm_g_a, m_ssm_lambda_re, m_ssm_lambda_im, m_ssm_b_re, m_ssm_b_im, m_ssm_c_re, m_ssm_c_im, m_ssm_d, m_ssm_log_dt, m_w_glu_b, m_b_glu_b, m_w_out_ab, m_pool_w, m_pool_scale, m_xa_wq, m_xa_wkv, m_xa_wo, m_ffn_w_up, m_ffn_conv, m_ffn_w_down, v_norm_mix_g, v_norm_xa_g, v_norm_ffn_g, v_norm_mem_g, v_norm_final_g, v_w_in_ab, v_conv_qkv_a, v_a_log_a, v_dt_bias_a, v_onorm_g_a, v_ssm_lambda_re, v_ssm_lambda_im, v_ssm_b_re, v_ssm_b_im, v_ssm_c_re, v_ssm_c_im, v_ssm_d, v_ssm_log_dt, v_w_glu_b, v_b_glu_b, v_w_out_ab, v_pool_w, v_pool_scale, v_xa_wq, v_xa_wkv, v_xa_wo, v_ffn_w_up, v_ffn_conv, v_ffn_w_down):
    args = (x, mem, norm_mix_g, norm_xa_g, norm_ffn_g, norm_mem_g, norm_final_g, w_in_ab, conv_qkv_a, a_log_a, dt_bias_a, onorm_g_a, ssm_lambda_re, ssm_lambda_im, ssm_b_re, ssm_b_im, ssm_c_re, ssm_c_im, ssm_d, ssm_log_dt, w_glu_b, b_glu_b, w_out_ab, pool_w, pool_scale, xa_wq, xa_wkv, xa_wo, ffn_w_up, ffn_conv, ffn_w_down, loss_target, m_norm_mix_g, m_norm_xa_g, m_norm_ffn_g, m_norm_mem_g, m_norm_final_g, m_w_in_ab, m_conv_qkv_a, m_a_log_a, m_dt_bias_a, m_onorm_g_a, m_ssm_lambda_re, m_ssm_lambda_im, m_ssm_b_re, m_ssm_b_im, m_ssm_c_re, m_ssm_c_im, m_ssm_d, m_ssm_log_dt, m_w_glu_b, m_b_glu_b, m_w_out_ab, m_pool_w, m_pool_scale, m_xa_wq, m_xa_wkv, m_xa_wo, m_ffn_w_up, m_ffn_conv, m_ffn_w_down, v_norm_mix_g, v_norm_xa_g, v_norm_ffn_g, v_norm_mem_g, v_norm_final_g, v_w_in_ab, v_conv_qkv_a, v_a_log_a, v_dt_bias_a, v_onorm_g_a, v_ssm_lambda_re, v_ssm_lambda_im, v_ssm_b_re, v_ssm_b_im, v_ssm_c_re, v_ssm_c_im, v_ssm_d, v_ssm_log_dt, v_w_glu_b, v_b_glu_b, v_w_out_ab, v_pool_w, v_pool_scale, v_xa_wq, v_xa_wkv, v_xa_wo, v_ffn_w_up, v_ffn_conv, v_ffn_w_down)
    return _step(dict(zip(INPUT_NAMES, args)))
```

```python
import functools
import math

import numpy as np
import jax
import jax.numpy as jnp
from jax import lax
from jax.experimental import pallas as pl
from jax.experimental.pallas import tpu as pltpu
from jax.experimental.pallas import tpu_sc as plsc

F32, BF16 = jnp.float32, jnp.bfloat16
HIGH, HIGHEST = lax.Precision.HIGH, lax.Precision.HIGHEST
MESH = pl.DeviceIdType.MESH

N_DEV = 8
SEQ, D_MODEL, MEM_LEN = 2048, 1024, 256
WIDTH_A, N_HEADS_A, HEAD_A, CONV_A = 512, 4, 128, 4
GDR_CHUNK = 128
SSM_WIDTH, SSM_GROUP, N_GROUPS, SSM_STATE = 512, 16, 32, 64
SSM_CH = N_GROUPS * SSM_STATE
SCAN_CB = 512
POOL_WINDOWS = (2, 4, 8, 16)
POOL_GROUP = 256
N_HEADS_X, HEAD_X = 4, 256
D_FF, CONV_FFN = 2816, 3
RMS_EPS = 1e-6
ADAM_LR, ADAM_B1, ADAM_B2, ADAM_EPS, ADAM_WD, ADAM_STEP = 0.001, 0.9, 0.999, 1e-08, 0.01, 10
LANE = 128
PACK_COLS = 1024
VMEM_LIMIT_BYTES = 56 * 1024 * 1024


def _params(sem=None):
    return pltpu.CompilerParams(dimension_semantics=sem, vmem_limit_bytes=VMEM_LIMIT_BYTES)


class Comm:
    def __init__(self, inputs, out_shapes, sems, start, end, mid=None):
        self.inputs, self.out_shapes, self.sems = list(inputs), list(out_shapes), list(sems)
        self.start, self.mid, self.end = start, mid, end


def _merge_comms(comms):
    comms = [c for c in comms if c is not None]
    if not comms:
        return None, []
    bounds, ni, no, ns = [], 0, 0, 0
    for c in comms:
        bounds.append((ni, no, ns))
        ni, no, ns = ni + len(c.inputs), no + len(c.out_shapes), ns + len(c.sems)

    def phase(which):
        def run(ins, outs, sems):
            for c, (i0, o0, s0) in zip(comms, bounds):
                fn = getattr(c, which)
                if fn is not None:
                    fn(ins[i0:i0 + len(c.inputs)], outs[o0:o0 + len(c.out_shapes)], sems[s0:s0 + len(c.sems)])
        return run

    merged = Comm([a for c in comms for a in c.inputs], [s for c in comms for s in c.out_shapes],
                  [s for c in comms for s in c.sems], phase("start"), phase("end"), phase("mid"))
    return merged, [(o0, o0 + len(c.out_shapes)) for c, (_, o0, _) in zip(comms, bounds)]


def _call(body, *, name, grid, in_specs, out_specs, out_shape, args, scratch_shapes=(), sem=None, comm=None):
    single = not isinstance(out_shape, (list, tuple))
    out_specs_l = [out_specs] if single else list(out_specs)
    out_shape_l = [out_shape] if single else list(out_shape)
    scratch_shapes = list(scratch_shapes)
    merged, spans = _merge_comms(comm if isinstance(comm, (list, tuple)) else [comm])
    if merged is None:
        outs = pl.pallas_call(body, name=name, grid=grid, in_specs=list(in_specs), out_specs=out_specs_l,
                              out_shape=out_shape_l, scratch_shapes=scratch_shapes, compiler_params=_params(sem))(*args)
        outs = outs[0] if single else outs
        return outs if comm is None else (outs, [])
    n_in, n_out, n_scr = len(in_specs), len(out_specs_l), len(scratch_shapes)
    ci, co = len(merged.inputs), len(merged.out_shapes)
    total = int(np.prod(grid))

    def wrapped(*refs):
        ins, cins = refs[:n_in], refs[n_in:n_in + ci]
        outs, couts = refs[n_in + ci:n_in + ci + n_out], refs[n_in + ci + n_out:n_in + ci + n_out + co]
        scr, csems = refs[n_in + ci + n_out + co:n_in + ci + n_out + co + n_scr], refs[n_in + ci + n_out + co + n_scr:]
        lin = pl.program_id(0)
        for d in range(1, len(grid)):
            lin = lin * grid[d] + pl.program_id(d)
        pl.when(lin == 0)(lambda: merged.start(cins, couts, csems))
        body(*ins, *outs, *scr)
        mid_step = min((3 * total) // 4, total - 1)
        pl.when(lin == mid_step)(lambda: merged.mid(cins, couts, csems))
        pl.when(lin == total - 1)(lambda: merged.end(cins, couts, csems))

    any_spec = pl.BlockSpec(memory_space=pl.ANY)
    res = pl.pallas_call(
        wrapped, name=name, grid=grid, in_specs=list(in_specs) + [any_spec] * ci,
        out_specs=out_specs_l + [any_spec] * co, out_shape=out_shape_l + merged.out_shapes,
        scratch_shapes=scratch_shapes + merged.sems,
        compiler_params=_params(("arbitrary",) * len(grid)))(*args, *merged.inputs)
    outs, couts = res[:n_out], res[n_out:]
    return (outs[0] if single else list(outs)), [list(couts[a:b]) for a, b in spans]


def _comm_only(comm, name):
    def body():
        pass

    _, couts = _call(body, name=name, grid=(1,), in_specs=[], out_specs=[], out_shape=[], args=[], comm=comm)
    return couts[0]


def _tile(dim, pref):
    best = None
    for t in range(LANE, min(dim, pref) + 1, LANE):
        if dim % t == 0:
            best = t
    return best if best is not None else dim


MM_VMEM_BUDGET = 40 * 1024 * 1024


def _mm_tiles(m, n, k, a_bytes, b_bytes, o_bytes, r_bytes):
    for tk in (k, _tile(k, 2048), _tile(k, 1024), _tile(k, 512)):
        for tm, tn in ((1024, 1536), (1024, 1024), (1024, 512), (512, 512), (256, 512), (256, 256)):
            tm, tn = _tile(m, tm), _tile(n, tn)
            acc = 0 if tk == k else tm * tn * 4
            need = 2 * (tm * tk * a_bytes + tk * tn * b_bytes + tm * tn * (o_bytes + r_bytes)) + acc
            if need <= MM_VMEM_BUDGET:
                return tm, tn, tk
    raise ValueError("no matmul tiling fits VMEM")


def _mm(a, b, mode, name, out_dtype=F32, res=None, comm=None):
    if mode == "nn":
        (m, k), n = a.shape, b.shape[1]
    elif mode == "nt":
        (m, k), n = a.shape, b.shape[0]
    else:
        (k, m), n = a.shape, b.shape[1]
    tm, tn, tk = _mm_tiles(m, n, k, a.dtype.itemsize, b.dtype.itemsize, jnp.dtype(out_dtype).itemsize,
                           0 if res is None else res.dtype.itemsize)
    nk = k // tk
    dims = {"nn": ((1,), (0,)), "nt": ((1,), (1,)), "tn": ((0,), (0,))}[mode]

    def body(*refs):
        if res is None:
            a_ref, b_ref, o_ref = refs[:3]
            r_ref = None
        else:
            a_ref, b_ref, r_ref, o_ref = refs[:4]
        part = lax.dot_general(a_ref[...].astype(BF16), b_ref[...].astype(BF16), (dims, ((), ())),
                               preferred_element_type=F32)

        def finish(out):
            if r_ref is not None:
                out = out + r_ref[...].astype(F32)
            o_ref[...] = out.astype(out_dtype)

        if nk == 1:
            finish(part)
            return
        acc = refs[-1]
        kk = pl.program_id(2)

        @pl.when(kk == 0)
        def _():
            acc[...] = part

        @pl.when(kk > 0)
        def _():
            acc[...] += part

        @pl.when(kk == nk - 1)
        def _():
            finish(acc[...])

    a_spec = (pl.BlockSpec((tk, tm), lambda i, j, q: (q, i)) if mode == "tn"
              else pl.BlockSpec((tm, tk), lambda i, j, q: (i, q)))
    b_spec = (pl.BlockSpec((tn, tk), lambda i, j, q: (j, q)) if mode == "nt"
              else pl.BlockSpec((tk, tn), lambda i, j, q: (q, j)))
    o_spec = pl.BlockSpec((tm, tn), lambda i, j, q: (i, j))
    in_specs, args = [a_spec, b_spec], [a, b]
    if res is not None:
        in_specs.append(o_spec)
        args.append(res)
    return _call(body, name=name, grid=(m // tm, n // tn, nk), in_specs=in_specs, out_specs=o_spec,
                 out_shape=jax.ShapeDtypeStruct((m, n), out_dtype),
                 scratch_shapes=[] if nk == 1 else [pltpu.VMEM((tm, tn), F32)],
                 sem=("parallel", "parallel", "arbitrary"), args=args, comm=comm)


def _mm_parts(parts, b, mode, name, out_dtype=F32):
    count = len(parts)
    rows, cols = parts[0].shape
    n = b.shape[1]
    if mode == "nn":
        tm, tn, tk = _mm_tiles(rows, n, count * cols, parts[0].dtype.itemsize, b.dtype.itemsize,
                               jnp.dtype(out_dtype).itemsize, 0)
        assert tk == count * cols

        def body(*refs):
            a_refs, b_refs, o_ref = refs[:count], refs[count:2 * count], refs[2 * count]
            out = None
            for a_ref, b_ref in zip(a_refs, b_refs):
                part = jnp.dot(a_ref[...].astype(BF16), b_ref[...].astype(BF16), preferred_element_type=F32)
                out = part if out is None else out + part
            o_ref[...] = out.astype(out_dtype)

        return _call(body, name=name, grid=(rows // tm, n // tn),
                     in_specs=[pl.BlockSpec((tm, cols), lambda i, j: (i, 0))] * count
                     + [pl.BlockSpec((cols, tn), functools.partial(lambda q, i, j: (q, j), q)) for q in range(count)],
                     out_specs=pl.BlockSpec((tm, tn), lambda i, j: (i, j)),
                     out_shape=jax.ShapeDtypeStruct((rows, n), out_dtype),
                     sem=("parallel", "parallel"), args=(*parts, *([b] * count)))
    tm, tn = _tile(cols, 1536), _tile(n, 1024)
    per = cols // tm

    def body_t(*refs):
        a_refs, b_ref, o_ref = refs[:count], refs[count], refs[count + 1]
        for q, a_ref in enumerate(a_refs):
            @pl.when(pl.program_id(0) // per == q)
            def _(a_ref=a_ref):
                o_ref[...] = lax.dot_general(a_ref[...].astype(BF16), b_ref[...].astype(BF16),
                                             (((0,), (0,)), ((), ())), preferred_element_type=F32).astype(out_dtype)

    return _call(body_t, name=name, grid=(count * per, n // tn),
                 in_specs=[pl.BlockSpec((rows, tm), functools.partial(lambda q, i, j: (0, jnp.clip(i - q * per, 0, per - 1)), q))
                           for q in range(count)] + [pl.BlockSpec((rows, tn), lambda i, j: (0, j))],
                 out_specs=pl.BlockSpec((tm, tn), lambda i, j: (i, j)),
                 out_shape=jax.ShapeDtypeStruct((count * cols, n), out_dtype),
                 sem=("parallel", "parallel"), args=(*parts, b))


def _mm_bd(a, b, mode, name, out_dtype=F32, res=None, comm=None, tm=1024):
    if mode == "tn":
        k = a.shape[0]
        nb = min(a.shape[1], b.shape[1]) // LANE
        ma, n = a.shape[1] // nb, b.shape[1] // nb

        def body(a_ref, b_ref, o_ref):
            o_ref[0] = lax.dot_general(a_ref[...].astype(BF16), b_ref[...].astype(BF16), (((0,), (0,)), ((), ())),
                                       preferred_element_type=F32).astype(out_dtype)

        return _call(body, name=name, grid=(nb,),
                     in_specs=[pl.BlockSpec((k, ma), lambda j: (0, j)), pl.BlockSpec((k, n), lambda j: (0, j))],
                     out_specs=pl.BlockSpec((1, ma, n), lambda j: (j, 0, 0)),
                     out_shape=jax.ShapeDtypeStruct((nb, ma, n), out_dtype), sem=("parallel",), args=(a, b), comm=comm)
    m = a.shape[0]
    nb = b.shape[0]
    ka = a.shape[1] // nb
    n = b.shape[2] if mode == "nn" else b.shape[1]
    tm = _tile(m, tm)
    dims = ((1,), (0,)) if mode == "nn" else ((1,), (1,))

    def body(*refs):
        if res is None:
            a_ref, b_ref, o_ref = refs
            r_ref = None
        else:
            a_ref, b_ref, r_ref, o_ref = refs
        out = lax.dot_general(a_ref[...].astype(BF16), b_ref[0].astype(BF16), (dims, ((), ())),
                              preferred_element_type=F32)
        if r_ref is not None:
            out = out + r_ref[...].astype(F32)
        o_ref[...] = out.astype(out_dtype)

    o_spec = pl.BlockSpec((tm, n), lambda i, j: (i, j))
    in_specs = [pl.BlockSpec((tm, ka), lambda i, j: (i, j)), pl.BlockSpec((1,) + b.shape[1:], lambda i, j: (j, 0, 0))]
    args = [a, b]
    if res is not None:
        in_specs.append(o_spec)
        args.append(res)
    return _call(body, name=name, grid=(m // tm, nb), in_specs=in_specs, out_specs=o_spec,
                 out_shape=jax.ShapeDtypeStruct((m, nb * n), out_dtype), sem=("parallel", "parallel"),
                 args=args, comm=comm)


def _rms_fwd(x, g, out_dtype, name, tr=512):
    rows, d = x.shape
    tr = min(tr, rows)

    def body(x_ref, g_ref, o_ref):
        xv = x_ref[...]
        r = lax.rsqrt(jnp.mean(xv * xv, axis=-1, keepdims=True) + RMS_EPS)
        o_ref[...] = (xv * r * g_ref[...]).astype(out_dtype)

    return pl.pallas_call(
        body, name=name, grid=(rows // tr,),
        in_specs=[pl.BlockSpec((tr, d), lambda i: (i, 0)), pl.BlockSpec((1, d), lambda i: (0, 0))],
        out_specs=pl.BlockSpec((tr, d), lambda i: (i, 0)), out_shape=jax.ShapeDtypeStruct((rows, d), out_dtype),
        compiler_params=_params(("parallel",)))(x, g.reshape(1, d))


def _rms_bwd(x, g, dy, dres, name, tr=512, comm=None):
    rows, d = x.shape
    tr = min(tr, rows)

    def body(*refs):
        if dres is None:
            x_ref, g_ref, dy_ref, dx_ref, dg_ref = refs
            r_ref = None
        else:
            x_ref, g_ref, dy_ref, r_ref, dx_ref, dg_ref = refs

        @pl.when(pl.program_id(0) == 0)
        def _():
            dg_ref[...] = jnp.zeros_like(dg_ref)

        xv, dyv = x_ref[...], dy_ref[...].astype(F32)
        r = lax.rsqrt(jnp.mean(xv * xv, axis=-1, keepdims=True) + RMS_EPS)
        xh = xv * r
        dyg = dyv * g_ref[...]
        dx = r * (dyg - xh * jnp.mean(dyg * xh, axis=-1, keepdims=True))
        if r_ref is not None:
            dx = dx + r_ref[...]
        dx_ref[...] = dx
        dg_ref[...] += jnp.sum(dyv * xh, axis=0, keepdims=True)

    blk = pl.BlockSpec((tr, d), lambda i: (i, 0))
    vec = pl.BlockSpec((1, d), lambda i: (0, 0))
    in_specs, args = [blk, vec, blk], [x, g.reshape(1, d), dy]
    if dres is not None:
        in_specs.append(blk)
        args.append(dres)
    return _call(
        body, name=name, grid=(rows // tr,), in_specs=in_specs, out_specs=[blk, vec],
        out_shape=[jax.ShapeDtypeStruct((rows, d), F32), jax.ShapeDtypeStruct((1, d), F32)],
        sem=("arbitrary",), args=args, comm=comm)


def _loss_head(x, g, target, name, tr=512):
    rows, d = x.shape

    def body(x_ref, g_ref, t_ref, loss_ref, dx_ref, dg_ref):
        @pl.when(pl.program_id(0) == 0)
        def _():
            dg_ref[...] = jnp.zeros_like(dg_ref)
            loss_ref[...] = jnp.zeros_like(loss_ref)

        xv = x_ref[...]
        r = lax.rsqrt(jnp.mean(xv * xv, axis=-1, keepdims=True) + RMS_EPS)
        xh = xv * r
        err = xh * g_ref[...] - t_ref[...]
        loss_ref[...] += 0.5 * jnp.sum(jnp.mean(err * err, axis=-1, keepdims=True), keepdims=True)
        dyv = err * (1.0 / d)
        dyg = dyv * g_ref[...]
        dx_ref[...] = r * (dyg - xh * jnp.mean(dyg * xh, axis=-1, keepdims=True))
        dg_ref[...] += jnp.sum(dyv * xh, axis=0, keepdims=True)

    blk = pl.BlockSpec((tr, d), lambda i: (i, 0))
    vec = pl.BlockSpec((1, d), lambda i: (0, 0))
    return pl.pallas_call(
        body, name=name, grid=(rows // tr,), in_specs=[blk, vec, blk],
        out_specs=[pl.BlockSpec((1, 1), lambda i: (0, 0)), blk, vec],
        out_shape=[jax.ShapeDtypeStruct((1, 1), F32), jax.ShapeDtypeStruct((rows, d), F32),
                   jax.ShapeDtypeStruct((1, d), F32)],
        compiler_params=_params(("arbitrary",)))(x, g.reshape(1, d), target)


def _shift_down(x, s):
    rows = lax.broadcasted_iota(jnp.int32, x.shape, 0)
    return jnp.where(rows >= s, pltpu.roll(x, s, 0), 0.0)


def _shift_up(x, s):
    n = x.shape[0]
    rows = lax.broadcasted_iota(jnp.int32, x.shape, 0)
    return jnp.where(rows < n - s, pltpu.roll(x, n - s, 0), 0.0)


def _sigmoid(x):
    return 1.0 / (1.0 + jnp.exp(-x))


def _silu_and_grad(x):
    s = _sigmoid(x)
    return x * s, s * (1.0 + x * (1.0 - s))


_GELU_C0, _GELU_C1 = math.sqrt(2.0 / math.pi), 0.044715


def _gelu_and_grad(x):
    th = jnp.tanh(_GELU_C0 * (x + _GELU_C1 * x * x * x))
    y = 0.5 * x * (1.0 + th)
    dy = 0.5 * (1.0 + th) + 0.5 * x * (1.0 - th * th) * _GELU_C0 * (1.0 + 3.0 * _GELU_C1 * x * x)
    return y, dy


def _ffn_act_fwd(h, w, name, tc=256, comm=None):
    t = h.shape[0]
    nb = D_FF // tc

    def body(hg_ref, hv_ref, wg_ref, wv_ref, a_ref):
        def conv(x, wr):
            return wr[2:3, :] * x + wr[1:2, :] * _shift_down(x, 1) + wr[0:1, :] * _shift_down(x, 2)

        cg = conv(hg_ref[...], wg_ref[...])
        cv = conv(hv_ref[...], wv_ref[...])
        a_ref[...] = (cg * _sigmoid(cg) * cv).astype(BF16)

    return _call(
        body, name=name, grid=(nb,),
        in_specs=[pl.BlockSpec((t, tc), lambda j: (0, j)), pl.BlockSpec((t, tc), lambda j: (0, j + nb)),
                  pl.BlockSpec((CONV_FFN, tc), lambda j: (0, j)), pl.BlockSpec((CONV_FFN, tc), lambda j: (0, j + nb))],
        out_specs=pl.BlockSpec((t, tc), lambda j: (0, j)), out_shape=jax.ShapeDtypeStruct((t, D_FF), BF16),
        sem=("parallel",), args=(h, h, w, w), comm=comm)


def _ffn_act_bwd(h, w, da, name, tc=256, comm=None):
    t = h.shape[0]
    nb = D_FF // tc

    def body(hg_ref, hv_ref, wg_ref, wv_ref, da_ref, dhg_ref, dhv_ref, dwg_ref, dwv_ref):
        hg, hv, wg, wv = hg_ref[...], hv_ref[...], wg_ref[...], wv_ref[...]
        hg1, hg2, hv1, hv2 = _shift_down(hg, 1), _shift_down(hg, 2), _shift_down(hv, 1), _shift_down(hv, 2)
        cg = wg[2:3, :] * hg + wg[1:2, :] * hg1 + wg[0:1, :] * hg2
        cv = wv[2:3, :] * hv + wv[1:2, :] * hv1 + wv[0:1, :] * hv2
        sg, dsg = _silu_and_grad(cg)
        dav = da_ref[...].astype(F32)
        dcv = dav * sg
        dcg = dav * cv * dsg

        def conv_t(dc, wr):
            return wr[2:3, :] * dc + wr[1:2, :] * _shift_up(dc, 1) + wr[0:1, :] * _shift_up(dc, 2)

        dhg_ref[...] = conv_t(dcg, wg).astype(BF16)
        dhv_ref[...] = conv_t(dcv, wv).astype(BF16)
        dwg_ref[0:1, :] = jnp.sum(dcg * hg2, axis=0, keepdims=True)
        dwg_ref[1:2, :] = jnp.sum(dcg * hg1, axis=0, keepdims=True)
        dwg_ref[2:3, :] = jnp.sum(dcg * hg, axis=0, keepdims=True)
        dwv_ref[0:1, :] = jnp.sum(dcv * hv2, axis=0, keepdims=True)
        dwv_ref[1:2, :] = jnp.sum(dcv * hv1, axis=0, keepdims=True)
        dwv_ref[2:3, :] = jnp.sum(dcv * hv, axis=0, keepdims=True)

    big = lambda off: pl.BlockSpec((t, tc), lambda j: (0, j + off))
    small = lambda off: pl.BlockSpec((CONV_FFN, tc), lambda j: (0, j + off))
    res = _call(
        body, name=name, grid=(nb,),
        in_specs=[big(0), big(nb), small(0), small(nb), big(0)],
        out_specs=[big(0), big(0), small(0), small(0)],
        out_shape=[jax.ShapeDtypeStruct((t, D_FF), BF16), jax.ShapeDtypeStruct((t, D_FF), BF16),
                   jax.ShapeDtypeStruct((CONV_FFN, D_FF), F32), jax.ShapeDtypeStruct((CONV_FFN, D_FF), F32)],
        sem=("parallel",), args=(h, h, w, w, da), comm=comm)
    (dhg, dhv, dwg, dwv), couts = res if comm is not None else (res, None)
    out = ((dhg, dhv), jnp.concatenate([dwg, dwv], axis=1))
    return out if comm is None else (out, couts)


def _attn_probs(q, k):
    s = lax.dot_general(q.astype(BF16), k.astype(BF16), (((1,), (1,)), ((), ())),
                        preferred_element_type=F32) * (HEAD_X ** -0.5)
    s = s - jnp.max(s, axis=-1, keepdims=True)
    p = jnp.exp(s)
    return p / jnp.sum(p, axis=-1, keepdims=True)


def _attn_fwd(q, kv, name, tq=512, comm=None):
    t = q.shape[0]

    def body(q_ref, k_ref, v_ref, o_ref):
        p = _attn_probs(q_ref[...], k_ref[...])
        o_ref[...] = jnp.dot(p.astype(BF16), v_ref[...].astype(BF16), preferred_element_type=F32).astype(BF16)

    return _call(
        body, name=name, grid=(N_HEADS_X, t // tq),
        in_specs=[pl.BlockSpec((tq, HEAD_X), lambda h, i: (i, h)),
                  pl.BlockSpec((MEM_LEN, HEAD_X), lambda h, i: (0, h)),
                  pl.BlockSpec((MEM_LEN, HEAD_X), lambda h, i: (0, h + N_HEADS_X))],
        out_specs=pl.BlockSpec((tq, HEAD_X), lambda h, i: (i, h)),
        out_shape=jax.ShapeDtypeStruct((t, N_HEADS_X * HEAD_X), BF16),
        sem=("parallel", "parallel"), args=(q, kv, kv), comm=comm)


def _attn_bwd(q, kv, do, name, tq=512):
    t = q.shape[0]

    def body(q_ref, k_ref, v_ref, do_ref, dq_ref, dk_ref, dv_ref):
        @pl.when(pl.program_id(1) == 0)
        def _():
            dk_ref[...] = jnp.zeros_like(dk_ref)
            dv_ref[...] = jnp.zeros_like(dv_ref)

        qb, kb, vb, dob = (r[...].astype(BF16) for r in (q_ref, k_ref, v_ref, do_ref))
        p = _attn_probs(qb, kb)
        dp = lax.dot_general(dob, vb, (((1,), (1,)), ((), ())), preferred_element_type=F32)
        ds = p * (dp - jnp.sum(dp * p, axis=-1, keepdims=True)) * (HEAD_X ** -0.5)
        dsb = ds.astype(BF16)
        dq_ref[...] = jnp.dot(dsb, kb, preferred_element_type=F32).astype(BF16)
        dk_ref[...] += lax.dot_general(dsb, qb, (((0,), (0,)), ((), ())), preferred_element_type=F32)
        dv_ref[...] += lax.dot_general(p.astype(BF16), dob, (((0,), (0,)), ((), ())), preferred_element_type=F32)

    qs = pl.BlockSpec((tq, HEAD_X), lambda h, i: (i, h))
    ms = pl.BlockSpec((MEM_LEN, HEAD_X), lambda h, i: (0, h))
    return pl.pallas_call(
        body, name=name, grid=(N_HEADS_X, t // tq),
        in_specs=[qs, ms, pl.BlockSpec((MEM_LEN, HEAD_X), lambda h, i: (0, h + N_HEADS_X)), qs],
        out_specs=[qs, ms, ms],
        out_shape=[jax.ShapeDtypeStruct((t, D_MODEL), BF16), jax.ShapeDtypeStruct((MEM_LEN, D_MODEL), F32),
                   jax.ShapeDtypeStruct((MEM_LEN, D_MODEL), F32)],
        compiler_params=_params(("parallel", "arbitrary")))(q, kv, kv, do)


def _pool_counts(t, win):
    pos = lax.broadcasted_iota(jnp.int32, (t, 1), 0).astype(F32) + 1.0
    return 1.0 / jnp.minimum(pos, float(win))


def _pool_delta(xv, win):
    s, step = xv, 1
    while step < win:
        s = s + _shift_down(s, step)
        step *= 2
    return s * _pool_counts(xv.shape[0], win) - xv


def _pool_delta_t(dv, win):
    s, step = dv * _pool_counts(dv.shape[0], win), 1
    while step < win:
        s = s + _shift_up(s, step)
        step *= 2
    return s - dv


def _pool_fwd(xn, w, scale, res, name):
    t = xn.shape[0]

    def make_branch(win, xn_ref, w_ref, s_ref, r_ref, o_ref):
        def branch():
            dl = _pool_delta(xn_ref[...], win)
            y = jnp.dot(dl.astype(BF16), w_ref[0], preferred_element_type=F32)
            o_ref[...] = r_ref[...] + y * s_ref[...]
        return branch

    def body(xn_ref, w_ref, s_ref, r_ref, o_ref):
        for gi, win in enumerate(POOL_WINDOWS):
            pl.when(pl.program_id(0) == gi)(make_branch(win, xn_ref, w_ref, s_ref, r_ref, o_ref))

    blk = pl.BlockSpec((t, POOL_GROUP), lambda g: (0, g))
    return pl.pallas_call(
        body, name=name, grid=(len(POOL_WINDOWS),),
        in_specs=[blk, pl.BlockSpec((1, POOL_GROUP, POOL_GROUP), lambda g: (g, 0, 0)),
                  pl.BlockSpec((1, POOL_GROUP), lambda g: (0, g)), blk],
        out_specs=blk, out_shape=jax.ShapeDtypeStruct((t, D_MODEL), F32),
        compiler_params=_params(("parallel",)))(xn, w, scale, res)


def _pool_bwd(xn, w, scale, dmix, name):
    t = xn.shape[0]

    def make_branch(win, xn_ref, w_ref, s_ref, d_ref, dxn_ref, dw_ref, ds_ref):
        def branch():
            dl = _pool_delta(xn_ref[...], win).astype(BF16)
            wv = w_ref[0]
            dm = d_ref[...]
            y = jnp.dot(dl, wv, preferred_element_type=F32)
            ds_ref[...] = jnp.sum(dm * y, axis=0, keepdims=True)
            dy = (dm * s_ref[...]).astype(BF16)
            dw_ref[0] = lax.dot_general(dl, dy, (((0,), (0,)), ((), ())), preferred_element_type=F32)
            ddl = lax.dot_general(dy, wv, (((1,), (1,)), ((), ())), preferred_element_type=F32)
            dxn_ref[...] = _pool_delta_t(ddl, win)
        return branch

    def body(*refs):
        for gi, win in enumerate(POOL_WINDOWS):
            pl.when(pl.program_id(0) == gi)(make_branch(win, *refs))

    blk = pl.BlockSpec((t, POOL_GROUP), lambda g: (0, g))
    wspec = pl.BlockSpec((1, POOL_GROUP, POOL_GROUP), lambda g: (g, 0, 0))
    vec = pl.BlockSpec((1, POOL_GROUP), lambda g: (0, g))
    return pl.pallas_call(
        body, name=name, grid=(len(POOL_WINDOWS),), in_specs=[blk, wspec, vec, blk], out_specs=[blk, wspec, vec],
        out_shape=[jax.ShapeDtypeStruct((t, D_MODEL), F32),
                   jax.ShapeDtypeStruct((len(POOL_WINDOWS), POOL_GROUP, POOL_GROUP), F32),
                   jax.ShapeDtypeStruct((1, D_MODEL), F32)],
        compiler_params=_params(("parallel",)))(xn, w, scale, dmix)


def _qkv_conv(h, wr):
    return (wr[3:4, :] * h + wr[2:3, :] * _shift_down(h, 1) + wr[1:2, :] * _shift_down(h, 2)
            + wr[0:1, :] * _shift_down(h, 3))


def _qkv_block_kind(j):
    return j < 2 * N_HEADS_A, jnp.where(j < N_HEADS_A, HEAD_A ** -0.5, 1.0)


def _qkv_pre_fwd(h, w, name):
    t, cols = h.shape

    def body(h_ref, w_ref, o_ref):
        normalised, scale = _qkv_block_kind(pl.program_id(0))
        c = _qkv_conv(h_ref[...], w_ref[...])
        s = c * _sigmoid(c)
        r = lax.rsqrt(jnp.sum(s * s, axis=-1, keepdims=True) + 1e-6)
        o_ref[...] = jnp.where(normalised, s * (r * scale), s)

    blk = pl.BlockSpec((t, HEAD_A), lambda j: (0, j))
    return pl.pallas_call(
        body, name=name, grid=(cols // HEAD_A,), in_specs=[blk, pl.BlockSpec((CONV_A, HEAD_A), lambda j: (0, j))],
        out_specs=blk, out_shape=jax.ShapeDtypeStruct((t, cols), F32), compiler_params=_params(("parallel",)))(h, w)


def _qkv_pre_bwd(h, w, dy, name):
    t, cols = h.shape

    def body(h_ref, w_ref, dy_ref, dh_ref, dw_ref):
        normalised, scale = _qkv_block_kind(pl.program_id(0))
        hv, wr, dyv = h_ref[...], w_ref[...], dy_ref[...]
        h1, h2, h3 = _shift_down(hv, 1), _shift_down(hv, 2), _shift_down(hv, 3)
        c = wr[3:4, :] * hv + wr[2:3, :] * h1 + wr[1:2, :] * h2 + wr[0:1, :] * h3
        s, dsilu = _silu_and_grad(c)
        r = lax.rsqrt(jnp.sum(s * s, axis=-1, keepdims=True) + 1e-6)
        y = s * r
        dys = dyv * scale
        ds = jnp.where(normalised, r * (dys - y * jnp.sum(dys * y, axis=-1, keepdims=True)), dyv)
        dc = ds * dsilu
        dh = (wr[3:4, :] * dc + wr[2:3, :] * _shift_up(dc, 1) + wr[1:2, :] * _shift_up(dc, 2)
              + wr[0:1, :] * _shift_up(dc, 3))
        dh_ref[...] = dh.astype(BF16)
        dw_ref[0:1, :] = jnp.sum(dc * h3, axis=0, keepdims=True)
        dw_ref[1:2, :] = jnp.sum(dc * h2, axis=0, keepdims=True)
        dw_ref[2:3, :] = jnp.sum(dc * h1, axis=0, keepdims=True)
        dw_ref[3:4, :] = jnp.sum(dc * hv, axis=0, keepdims=True)

    blk = pl.BlockSpec((t, HEAD_A), lambda j: (0, j))
    taps = pl.BlockSpec((CONV_A, HEAD_A), lambda j: (0, j))
    return pl.pallas_call(
        body, name=name, grid=(cols // HEAD_A,), in_specs=[blk, taps, blk], out_specs=[blk, taps],
        out_shape=[jax.ShapeDtypeStruct((t, cols), BF16), jax.ShapeDtypeStruct((CONV_A, cols), F32)],
        compiler_params=_params(("parallel",)))(h, w, dy)


def _softplus(x):
    return jnp.maximum(x, 0.0) + jnp.log1p(jnp.exp(-jnp.abs(x)))


def _gates_fwd(ba, arow, brow, name):
    t = ba.shape[0]

    def body(x_ref, a_ref, b_ref, o_ref):
        xv = x_ref[...]
        lane = lax.broadcasted_iota(jnp.int32, xv.shape, 1)
        beta = _sigmoid(xv)
        g = -jnp.exp(a_ref[...]) * _softplus(xv + b_ref[...])
        o_ref[...] = jnp.where(lane < N_HEADS_A, beta, jnp.where(lane < 2 * N_HEADS_A, g, 0.0))

    return pl.pallas_call(body, name=name, out_shape=jax.ShapeDtypeStruct((t, LANE), F32),
                          compiler_params=_params())(ba, arow, brow)


def _gates_bwd(ba, arow, brow, dgb, name):
    t = ba.shape[0]

    def body(x_ref, a_ref, b_ref, d_ref, dx_ref, da_ref, db_ref):
        xv = x_ref[...]
        dv = d_ref[0] + d_ref[1] + d_ref[2] + d_ref[3]
        lane = lax.broadcasted_iota(jnp.int32, xv.shape, 1)
        beta = _sigmoid(xv)
        ea = jnp.exp(a_ref[...])
        z = xv + b_ref[...]
        dgv = jnp.where((lane >= N_HEADS_A) & (lane < 2 * N_HEADS_A), dv, 0.0) * (-ea)
        dz = dgv * _sigmoid(z)
        dx = jnp.where(lane < N_HEADS_A, dv * beta * (1.0 - beta), dz)
        dx_ref[...] = dx.astype(BF16)
        db_ref[...] = jnp.sum(dz, axis=0, keepdims=True)
        da_ref[...] = jnp.sum(dgv * _softplus(z), axis=0, keepdims=True)

    return pl.pallas_call(
        body, name=name,
        out_shape=[jax.ShapeDtypeStruct((t, LANE), BF16), jax.ShapeDtypeStruct((1, LANE), F32),
                   jax.ShapeDtypeStruct((1, LANE), F32)],
        compiler_params=_params())(ba, arow, brow, dgb)


def _head_gates(gates, head):
    lane = lax.broadcasted_iota(jnp.int32, gates.shape, 1)
    beta = jnp.sum(jnp.where(lane == head, gates, 0.0), axis=1, keepdims=True)
    g = jnp.sum(jnp.where(lane == head + N_HEADS_A, gates, 0.0), axis=1, keepdims=True)
    return beta, g


_B_NN, _B_NT, _B_TN = ((2,), (1,)), ((2,), (2,)), ((1,), (1,))


def _bdot(a, b, dims=_B_NN, prec=None):
    if prec is None:
        a, b = a.astype(BF16), b.astype(BF16)
    return lax.dot_general(a, b, (dims, ((0,), (0,))), precision=prec, preferred_element_type=F32)


def _heads_of(ref):
    return jnp.stack([ref[:, h * HEAD_A:(h + 1) * HEAD_A] for h in range(N_HEADS_A)])


def _all_head_gates(gates):
    pairs = [_head_gates(gates, h) for h in range(N_HEADS_A)]
    return jnp.stack([b for b, _ in pairs]), jnp.stack([g for _, g in pairs])


def _gdr_terms(k, beta, g):
    h, c = k.shape[0], GDR_CHUNK
    row = lax.broadcasted_iota(jnp.int32, (c, c), 0)
    col = lax.broadcasted_iota(jnp.int32, (c, c), 1)
    causal, strict = row >= col, row > col
    lower = jnp.broadcast_to(causal.astype(F32), (h, c, c))
    gcum = _bdot(lower, jnp.broadcast_to(g, (h, c, c)), prec=HIGHEST)
    diff = gcum - jnp.swapaxes(gcum, 1, 2)
    decay = jnp.where(causal, jnp.exp(jnp.where(causal, diff, 0.0)), 0.0)
    kb = k * beta
    return row, col, causal, strict, gcum, decay, kb, _bdot(kb, k, _B_NT)


def _unit_lower_inverses(a):
    c = a.shape[1]
    eye = (lax.broadcasted_iota(jnp.int32, (c, c), 0) == lax.broadcasted_iota(jnp.int32, (c, c), 1)).astype(F32)
    p = -a
    inv = eye + p
    step = 1
    while 2 * step < c:
        p = _bdot(p, p, prec=HIGH)
        inv = inv + _bdot(inv, p, prec=HIGH)
        step *= 2
    return inv


def _gdr_fwd(qkv, gates, name, comm=None):
    t = qkv.shape[0]
    c, nh = GDR_CHUNK, N_HEADS_A
    n = t // c

    def body(q_ref, k_ref, v_ref, gb_ref, o_ref, tm_ref, s_ref, state):
        @pl.when(pl.program_id(0) == 0)
        def _():
            state[...] = jnp.zeros_like(state)

        qv, kv, vv = _heads_of(q_ref), _heads_of(k_ref), _heads_of(v_ref)
        beta, g = _all_head_gates(gb_ref[...])
        row, col, causal, strict, gcum, decay, kb, kk = _gdr_terms(kv, beta, g)
        tm = _unit_lower_inverses(jnp.where(strict, kk * decay, 0.0))
        e = jnp.exp(gcum)
        u = _bdot(tm, vv * beta, prec=HIGH)
        w = _bdot(tm, kb * e, prec=HIGH)
        p = jnp.where(causal, _bdot(qv, kv, _B_NT) * decay, 0.0)
        s = state[...]
        s_ref[:, 0] = s
        tm_ref[:, 0] = tm
        vn = u - _bdot(w, s)
        o = _bdot(qv * e, s) + _bdot(p, vn)
        for h in range(nh):
            o_ref[:, h * HEAD_A:(h + 1) * HEAD_A] = o[h]
        glast = gcum[:, c - 1:c, :]
        state[...] = s * jnp.exp(glast) + _bdot(kv * jnp.exp(glast - gcum), vn, _B_TN)

    part = lambda p: pl.BlockSpec((c, WIDTH_A), lambda i: (i, p))
    mat = pl.BlockSpec((nh, 1, c, c), lambda i: (0, i, 0, 0))
    return _call(
        body, name=name, grid=(n,), in_specs=[part(0), part(1), part(2), pl.BlockSpec((c, LANE), lambda i: (i, 0))],
        out_specs=[part(0), mat, mat],
        out_shape=[jax.ShapeDtypeStruct((t, WIDTH_A), F32), jax.ShapeDtypeStruct((nh, n, c, c), F32),
                   jax.ShapeDtypeStruct((nh, n, HEAD_A, HEAD_A), F32)],
        scratch_shapes=[pltpu.VMEM((nh, HEAD_A, HEAD_A), F32)], sem=("arbitrary",),
        args=(qkv, qkv, qkv, gates), comm=comm)


def _gdr_bwd(qkv, gates, tm_all, s_all, do, name, comm=None):
    t = qkv.shape[0]
    c, nh = GDR_CHUNK, N_HEADS_A
    n = t // c

    def body(q_ref, k_ref, v_ref, gb_ref, tm_ref, s_ref, do_ref, dqkv_ref, dgb_ref, dstate):
        @pl.when(pl.program_id(0) == 0)
        def _():
            dstate[...] = jnp.zeros_like(dstate)

        qv, kv, vv, dov = _heads_of(q_ref), _heads_of(k_ref), _heads_of(v_ref), _heads_of(do_ref)
        beta, g = _all_head_gates(gb_ref[...])
        tm, s, dsp = tm_ref[:, 0], s_ref[:, 0], dstate[...]
        row, col, causal, strict, gcum, decay, kb, kk = _gdr_terms(kv, beta, g)
        rowsum = lambda x: jnp.sum(x, axis=2, keepdims=True)
        e = jnp.exp(gcum)
        vb, kbe = vv * beta, kb * e
        u = _bdot(tm, vb, prec=HIGH)
        w = _bdot(tm, kbe, prec=HIGH)
        qk = _bdot(qv, kv, _B_NT)
        p = jnp.where(causal, qk * decay, 0.0)
        vn = u - _bdot(w, s)
        glast = gcum[:, c - 1:c, :]
        el = jnp.exp(glast)
        f = jnp.exp(glast - gcum)
        kd = kv * f
        qe = qv * e

        dvn = _bdot(p, dov, _B_TN) + _bdot(kd, dsp)
        dglast = el[:, :, 0:1] * jnp.sum(s * dsp, axis=(1, 2), keepdims=True)
        dkd = _bdot(vn, dsp, _B_NT)
        dk = dkd * f
        df = rowsum(dkd * kv) * f[:, :, 0:1]
        dglast = dglast + jnp.sum(df, axis=1, keepdims=True)
        dgc = -df
        dp = jnp.where(causal, _bdot(dov, vn, _B_NT), 0.0)
        dqe = _bdot(dov, s, _B_NT)
        dq = dqe * e
        de = rowsum(dqe * qv)
        dstate[...] = dsp * el + _bdot(qe, dov, _B_TN) - _bdot(w, dvn, _B_TN)
        dw = -_bdot(dvn, s, _B_NT)
        dvb = _bdot(tm, dvn, _B_TN, prec=HIGH)
        dkbe = _bdot(tm, dw, _B_TN, prec=HIGH)
        da = -jnp.where(strict, _bdot(dvb, u, _B_NT) + _bdot(dkbe, w, _B_NT), 0.0)
        dkk = da * decay
        dqk = dp * decay
        dd = da * kk + dp * qk
        dq = dq + _bdot(dqk, kv)
        dk = dk + _bdot(dqk, qv, _B_TN)
        dkb = _bdot(dkk, kv) + dkbe * e
        dk = dk + _bdot(dkk, kb, _B_TN)
        de = de + rowsum(dkbe * kb)
        dk = dk + dkb * beta
        dbeta = rowsum(dkb * kv) + rowsum(dvb * vv)
        m = dd * decay
        dgc = dgc + rowsum(m) - rowsum(jnp.swapaxes(m, 1, 2))
        dgc = dgc + de * e[:, :, 0:1]
        dgc = dgc + jnp.where(row[:, 0:1] == c - 1, dglast, 0.0)
        upper = jnp.broadcast_to((row <= col).astype(F32), (nh, c, c))
        dg = _bdot(upper, jnp.broadcast_to(dgc, (nh, c, c)), prec=HIGHEST)
        dv = dvb * beta
        for p, grad in enumerate((dq, dk, dv)):
            for h in range(nh):
                dqkv_ref[:, p * WIDTH_A + h * HEAD_A:p * WIDTH_A + (h + 1) * HEAD_A] = grad[h]
        head = lax.broadcasted_iota(jnp.int32, (nh, c, LANE), 0)
        lane = lax.broadcasted_iota(jnp.int32, (nh, c, LANE), 2)
        dgb_ref[...] = jnp.where(lane == head, dbeta, jnp.where(lane == head + nh, dg, 0.0))

    part = lambda p: pl.BlockSpec((c, WIDTH_A), lambda i: (n - 1 - i, p))
    mat = pl.BlockSpec((nh, 1, c, c), lambda i: (0, n - 1 - i, 0, 0))
    return _call(
        body, name=name, grid=(n,),
        in_specs=[part(0), part(1), part(2), pl.BlockSpec((c, LANE), lambda i: (n - 1 - i, 0)), mat, mat, part(0)],
        out_specs=[pl.BlockSpec((c, 3 * WIDTH_A), lambda i: (n - 1 - i, 0)),
                   pl.BlockSpec((nh, c, LANE), lambda i: (0, n - 1 - i, 0))],
        out_shape=[jax.ShapeDtypeStruct((t, 3 * WIDTH_A), F32), jax.ShapeDtypeStruct((nh, t, LANE), F32)],
        scratch_shapes=[pltpu.VMEM((nh, HEAD_A, HEAD_A), F32)], sem=("arbitrary",),
        args=(qkv, qkv, qkv, gates, tm_all, s_all, do), comm=comm)


def _onorm_fwd(o, gate, g, name):
    t = o.shape[0]

    def body(o_ref, gate_ref, g_ref, y_ref):
        ov, gv = o_ref[...], gate_ref[...]
        r = lax.rsqrt(jnp.mean(ov * ov, axis=-1, keepdims=True) + RMS_EPS)
        y_ref[...] = (ov * r * g_ref[...] * gv * _sigmoid(gv)).astype(BF16)

    blk = pl.BlockSpec((t, HEAD_A), lambda j: (0, j))
    return pl.pallas_call(
        body, name=name, grid=(N_HEADS_A,), in_specs=[blk, blk, pl.BlockSpec((1, HEAD_A), lambda j: (0, 0))],
        out_specs=blk, out_shape=jax.ShapeDtypeStruct((t, WIDTH_A), BF16),
        compiler_params=_params(("parallel",)))(o, gate, g)


def _onorm_bwd(o, gate, g, dy, name):
    t = o.shape[0]

    def body(o_ref, gate_ref, g_ref, dy_ref, do_ref, dgate_ref, dg_ref):
        @pl.when(pl.program_id(0) == 0)
        def _():
            dg_ref[...] = jnp.zeros_like(dg_ref)

        ov, gv, dyv = o_ref[...], gate_ref[...], dy_ref[...].astype(F32)
        r = lax.rsqrt(jnp.mean(ov * ov, axis=-1, keepdims=True) + RMS_EPS)
        oh = ov * r
        sg, dsg = _silu_and_grad(gv)
        dgate_ref[...] = (dyv * oh * g_ref[...] * dsg).astype(BF16)
        dn = dyv * sg
        dg_ref[...] += jnp.sum(dn * oh, axis=0, keepdims=True)
        dng = dn * g_ref[...]
        do_ref[...] = r * (dng - oh * jnp.mean(dng * oh, axis=-1, keepdims=True))

    blk = pl.BlockSpec((t, HEAD_A), lambda j: (0, j))
    vec = pl.BlockSpec((1, HEAD_A), lambda j: (0, 0))
    return pl.pallas_call(
        body, name=name, grid=(N_HEADS_A,), in_specs=[blk, blk, vec, blk], out_specs=[blk, blk, vec],
        out_shape=[jax.ShapeDtypeStruct((t, WIDTH_A), F32), jax.ShapeDtypeStruct((t, WIDTH_A), BF16),
                   jax.ShapeDtypeStruct((1, HEAD_A), F32)],
        compiler_params=_params(("arbitrary",)))(o, gate, g, dy)


def _cmul(ar, ai, br, bi):
    return ar * br - ai * bi, ar * bi + ai * br


def _scan_tables(ar, ai, reverse):
    p1 = (ar, ai)
    p2 = _cmul(*p1, *p1)
    p4 = _cmul(*p2, *p2)
    p8 = _cmul(*p4, *p4)
    p3 = _cmul(*p2, *p1)
    p5 = _cmul(*p4, *p1)
    p6 = _cmul(*p4, *p2)
    p7 = _cmul(*p4, *p3)
    pows = [p1, p2, p3, p4, p5, p6, p7, p8]
    rows = lax.broadcasted_iota(jnp.int32, (8, ar.shape[1]), 0)
    tr = jnp.zeros((8, ar.shape[1]), F32)
    ti = jnp.zeros((8, ar.shape[1]), F32)
    for r in range(8):
        pw = pows[7 - r] if reverse else pows[r]
        tr = jnp.where(rows == r, pw[0], tr)
        ti = jnp.where(rows == r, pw[1], ti)
    return p1, p2, p4, p8, tr, ti


def _tile_scan(xr, xi, p1, p2, p4, reverse):
    rows = lax.broadcasted_iota(jnp.int32, xr.shape, 0)
    for s, (pr, pi) in ((1, p1), (2, p2), (4, p4)):
        if reverse:
            keep = rows < 8 - s
            sr, si = pltpu.roll(xr, 8 - s, 0), pltpu.roll(xi, 8 - s, 0)
        else:
            keep = rows >= s
            sr, si = pltpu.roll(xr, s, 0), pltpu.roll(xi, s, 0)
        sr, si = jnp.where(keep, sr, 0.0), jnp.where(keep, si, 0.0)
        mr, mi = _cmul(pr, pi, sr, si)
        xr, xi = xr + mr, xi + mi
    return xr, xi


def _s5_scan_fwd(bu, a, name, tb=512, comm=None):
    t = bu.shape[0]
    cb = SCAN_CB
    nt = t // tb

    def body(b_ref, a_ref, x_ref, carry):
        @pl.when(pl.program_id(1) == 0)
        def _():
            carry[...] = jnp.zeros_like(carry)

        ar, ai = a_ref[:, 0:cb], a_ref[:, cb:2 * cb]
        p1, p2, p4, p8, tr, ti = _scan_tables(ar, ai, False)

        def step(j, c):
            cr, ci = c
            i = pl.multiple_of(j * 8, 8)
            xr, xi = _tile_scan(b_ref[pl.ds(i, 8), 0:cb], b_ref[pl.ds(i, 8), cb:2 * cb], p1, p2, p4, False)
            mr, mi = _cmul(tr, ti, cr, ci)
            xr, xi = xr + mr, xi + mi
            x_ref[pl.ds(i, 8), 0:cb] = xr
            x_ref[pl.ds(i, 8), cb:2 * cb] = xi
            return xr[7:8, :], xi[7:8, :]

        cr, ci = lax.fori_loop(0, tb // 8, step, (carry[0:1, :], carry[1:2, :]), unroll=2)
        carry[0:1, :] = cr
        carry[1:2, :] = ci

    blk = pl.BlockSpec((tb, 2 * cb), lambda j, i: (i, j))
    return _call(
        body, name=name, grid=(SSM_CH // cb, nt),
        in_specs=[blk, pl.BlockSpec((1, 2 * cb), lambda j, i: (0, j))], out_specs=blk,
        out_shape=jax.ShapeDtypeStruct((t, 2 * SSM_CH), F32), scratch_shapes=[pltpu.VMEM((8, cb), F32)],
        sem=("parallel", "arbitrary"), args=(bu, a), comm=comm)


def _s5_scan_bwd(dx, x, a, name, tb=512, comm=None):
    t = dx.shape[0]
    cb = SCAN_CB
    nt = t // tb
    nj = tb // 8

    def body(d_ref, x_ref, xp_ref, a_ref, l_ref, da_ref, carry, acc):
        tblk = pl.program_id(1)

        @pl.when(tblk == 0)
        def _():
            carry[...] = jnp.zeros_like(carry)
            acc[...] = jnp.zeros_like(acc)

        ar, ai = a_ref[:, 0:cb], a_ref[:, cb:2 * cb]
        p1, p2, p4, p8, tr, ti = _scan_tables(ar, -ai, True)
        rows = lax.broadcasted_iota(jnp.int32, (8, cb), 0)

        def step(jj, c):
            cr, ci, sr_acc, si_acc = c
            j = nj - 1 - jj
            i = pl.multiple_of(j * 8, 8)
            lr, li = _tile_scan(d_ref[pl.ds(i, 8), 0:cb], d_ref[pl.ds(i, 8), cb:2 * cb], p1, p2, p4, True)
            mr, mi = _cmul(tr, ti, cr, ci)
            lr, li = lr + mr, li + mi
            l_ref[pl.ds(i, 8), 0:cb] = lr
            l_ref[pl.ds(i, 8), cb:2 * cb] = li
            ip = pl.multiple_of(jnp.maximum(j - 1, 0) * 8, 8)
            prev_r = jnp.where(j > 0, x_ref[pl.ds(ip, 8), 0:cb], xp_ref[:, 0:cb])
            prev_i = jnp.where(j > 0, x_ref[pl.ds(ip, 8), cb:2 * cb], xp_ref[:, cb:2 * cb])
            edge = jnp.where(jnp.logical_and(j == 0, tblk == nt - 1), 0.0, 1.0)
            xs_r = jnp.where(rows == 0, pltpu.roll(prev_r, 1, 0) * edge, pltpu.roll(x_ref[pl.ds(i, 8), 0:cb], 1, 0))
            xs_i = jnp.where(rows == 0, pltpu.roll(prev_i, 1, 0) * edge, pltpu.roll(x_ref[pl.ds(i, 8), cb:2 * cb], 1, 0))
            sr_acc = sr_acc + lr * xs_r + li * xs_i
            si_acc = si_acc + li * xs_r - lr * xs_i
            return lr[0:1, :], li[0:1, :], sr_acc, si_acc

        cr, ci, sr_acc, si_acc = lax.fori_loop(
            0, nj, step, (carry[0:1, :], carry[1:2, :], acc[:, 0:cb], acc[:, cb:2 * cb]))
        carry[0:1, :] = cr
        carry[1:2, :] = ci
        acc[:, 0:cb] = sr_acc
        acc[:, cb:2 * cb] = si_acc

        @pl.when(tblk == nt - 1)
        def _():
            da_ref[...] = jnp.sum(acc[...], axis=0, keepdims=True)

    blk = pl.BlockSpec((tb, 2 * cb), lambda j, i: (nt - 1 - i, j))
    prev = pl.BlockSpec((8, 2 * cb), lambda j, i: (jnp.maximum((nt - 1 - i) * (tb // 8) - 1, 0), j))
    vec = pl.BlockSpec((1, 2 * cb), lambda j, i: (0, j))
    return _call(
        body, name=name, grid=(SSM_CH // cb, nt), in_specs=[blk, blk, prev, vec], out_specs=[blk, vec],
        out_shape=[jax.ShapeDtypeStruct((t, 2 * SSM_CH), F32), jax.ShapeDtypeStruct((1, 2 * SSM_CH), F32)],
        scratch_shapes=[pltpu.VMEM((8, cb), F32), pltpu.VMEM((8, 2 * cb), F32)],
        sem=("parallel", "arbitrary"), args=(dx, x, x, a), comm=comm)


def _glu_fwd(yc, u, dvec, wg, bg, name, tr=512):
    t = yc.shape[0]

    def body(yc_ref, u_ref, d_ref, w_ref, b_ref, yl_ref, yb_ref):
        yl = yc_ref[...] + d_ref[...] * u_ref[...]
        yl_ref[...] = yl
        yg, _ = _gelu_and_grad(yl)
        z = jnp.dot(yg.astype(BF16), w_ref[...], preferred_element_type=F32) + b_ref[...]
        yb_ref[...] = (yg * _sigmoid(z)).astype(BF16)

    blk = pl.BlockSpec((tr, SSM_WIDTH), lambda i: (i, 0))
    vec = pl.BlockSpec((1, SSM_WIDTH), lambda i: (0, 0))
    return pl.pallas_call(
        body, name=name, grid=(t // tr,),
        in_specs=[blk, blk, vec, pl.BlockSpec((SSM_WIDTH, SSM_WIDTH), lambda i: (0, 0)), vec],
        out_specs=[blk, blk],
        out_shape=[jax.ShapeDtypeStruct((t, SSM_WIDTH), F32), jax.ShapeDtypeStruct((t, SSM_WIDTH), BF16)],
        compiler_params=_params(("parallel",)))(yc, u, dvec, wg, bg)


def _glu_bwd(yl, u, dvec, wg, bg, dyb, name, tr=512):
    t = yl.shape[0]

    def body(yl_ref, u_ref, d_ref, w_ref, b_ref, dy_ref, dyl_ref, du_ref, dw_ref, db_ref, dd_ref):
        @pl.when(pl.program_id(0) == 0)
        def _():
            dw_ref[...] = jnp.zeros_like(dw_ref)
            db_ref[...] = jnp.zeros_like(db_ref)
            dd_ref[...] = jnp.zeros_like(dd_ref)

        ylv, dyv, wv = yl_ref[...], dy_ref[...].astype(F32), w_ref[...]
        yg, dgelu = _gelu_and_grad(ylv)
        ygb = yg.astype(BF16)
        z = jnp.dot(ygb, wv, preferred_element_type=F32) + b_ref[...]
        sg = _sigmoid(z)
        dz = dyv * yg * sg * (1.0 - sg)
        dzb = dz.astype(BF16)
        dyg = dyv * sg + lax.dot_general(dzb, wv, (((1,), (1,)), ((), ())), preferred_element_type=F32)
        dyl = dyg * dgelu
        dyl_ref[...] = dyl.astype(BF16)
        du_ref[...] = dyl * d_ref[...]
        dw_ref[...] += lax.dot_general(ygb, dzb, (((0,), (0,)), ((), ())), preferred_element_type=F32)
        db_ref[...] += jnp.sum(dz, axis=0, keepdims=True)
        dd_ref[...] += jnp.sum(dyl * u_ref[...], axis=0, keepdims=True)

    blk = pl.BlockSpec((tr, SSM_WIDTH), lambda i: (i, 0))
    vec = pl.BlockSpec((1, SSM_WIDTH), lambda i: (0, 0))
    wsp = pl.BlockSpec((SSM_WIDTH, SSM_WIDTH), lambda i: (0, 0))
    return pl.pallas_call(
        body, name=name, grid=(t // tr,), in_specs=[blk, blk, vec, wsp, vec, blk],
        out_specs=[blk, blk, wsp, vec, vec],
        out_shape=[jax.ShapeDtypeStruct((t, SSM_WIDTH), BF16), jax.ShapeDtypeStruct((t, SSM_WIDTH), F32),
                   jax.ShapeDtypeStruct((SSM_WIDTH, SSM_WIDTH), F32), jax.ShapeDtypeStruct((1, SSM_WIDTH), F32),
                   jax.ShapeDtypeStruct((1, SSM_WIDTH), F32)],
        compiler_params=_params(("arbitrary",)))(yl, u, dvec, wg, bg, dyb)


def _mesh_pos():
    return lax.axis_index("x"), lax.axis_index("y"), lax.axis_index("c")


def _device_index():
    x, y, c = _mesh_pos()
    return 4 * x + 2 * y + c


def _gather_comm(arrays):
    na = len(arrays)

    def own_copy(ins, outs, sems, ai):
        return pltpu.make_async_copy(ins[ai], outs[ai].at[_device_index()], sems[2].at[ai])

    def ctx(ins, outs, sems):
        send_sems, recv_sems = sems[:2]
        x, y, c = _mesh_pos()
        chips = [(1 - x, y), (x, 1 - y), (1 - x, 1 - y)]

        def copy(ai, kk, block, to, own=False):
            slot = outs[ai].at[4 * block[0] + 2 * block[1] + block[2]]
            return pltpu.make_async_remote_copy(
                src_ref=ins[ai] if own else slot, dst_ref=slot, send_sem=send_sems.at[ai, kk],
                recv_sem=recv_sems.at[ai, kk], device_id=to, device_id_type=MESH)

        return (x, y, c), (x, y, 1 - c), chips, c, copy

    def start(ins, outs, sems):
        me, sibling, chips, c, copy = ctx(ins, outs, sems)
        for ai in range(na):
            copy(ai, 0, me, sibling, own=True).start()
            for j, chip in enumerate(chips):
                copy(ai, 1 + j, me, (*chip, c), own=True).start()
        for ai in range(na):
            own_copy(ins, outs, sems, ai).start()

    def mid(ins, outs, sems):
        me, sibling, chips, c, copy = ctx(ins, outs, sems)
        for ai in range(na):
            for j, chip in enumerate(chips):
                copy(ai, 1 + j, (*chip, c), me).wait_recv()
                copy(ai, 4 + j, (*chip, c), sibling).start()

    def end(ins, outs, sems):
        me, sibling, chips, c, copy = ctx(ins, outs, sems)
        for ai in range(na):
            copy(ai, 0, sibling, me).wait_recv()
            copy(ai, 0, me, sibling, own=True).wait_send()
            for j, chip in enumerate(chips):
                copy(ai, 4 + j, (*chip, 1 - c), me).wait_recv()
                copy(ai, 1 + j, me, (*chip, c), own=True).wait_send()
                copy(ai, 4 + j, (*chip, c), sibling).wait_send()
            own_copy(ins, outs, sems, ai).wait()

    return Comm(arrays, [jax.ShapeDtypeStruct((N_DEV,) + a.shape, a.dtype) for a in arrays],
                [pltpu.SemaphoreType.DMA((na, 7)), pltpu.SemaphoreType.DMA((na, 7)), pltpu.SemaphoreType.DMA((na,))],
                start, end, mid)


def _sequencer_gather(arrays, name, collective_id):
    comm = _gather_comm(arrays)
    na = len(arrays)

    def body(*refs):
        ins, outs, sems = refs[:na], refs[na:2 * na], refs[2 * na:]
        x, y, c = _mesh_pos()
        peers = [(x, y, 1 - c), (1 - x, y, c), (x, 1 - y, c), (1 - x, 1 - y, c)]
        barrier = pltpu.get_barrier_semaphore()
        for peer in peers:
            pl.semaphore_signal(barrier, inc=1, device_id=peer, device_id_type=MESH)
        pl.semaphore_wait(barrier, len(peers))
        comm.start(ins, outs, sems)
        comm.mid(ins, outs, sems)
        comm.end(ins, outs, sems)

    return list(pl.kernel(
        body, out_type=tuple(comm.out_shapes), mesh=plsc.ScalarSubcoreMesh(axis_name="sequencer", num_cores=1),
        name=name, scratch_types=tuple(comm.sems),
        compiler_params=pltpu.CompilerParams(collective_id=collective_id))(*arrays))


def _sequencer_exchange(comm, peers_of, name, collective_id):
    na = len(comm.inputs)

    def body(*refs):
        ins, outs, sems = refs[:na], refs[na:na + len(comm.out_shapes)], refs[na + len(comm.out_shapes):]
        peers = peers_of(*_mesh_pos())
        barrier = pltpu.get_barrier_semaphore()
        for peer in peers:
            pl.semaphore_signal(barrier, inc=1, device_id=peer, device_id_type=MESH)
        pl.semaphore_wait(barrier, len(peers))
        comm.start(ins, outs, sems)
        comm.end(ins, outs, sems)

    return list(pl.kernel(
        body, out_type=tuple(comm.out_shapes), mesh=plsc.ScalarSubcoreMesh(axis_name="sequencer", num_cores=1),
        name=name, scratch_types=tuple(comm.sems),
        compiler_params=pltpu.CompilerParams(collective_id=collective_id))(*comm.inputs))


SIBLING_SWAP_ID, CHIP_EXCHANGE_ID = 9, 10


def _sequencer_swap(arrays, name):
    return _sequencer_exchange(_swap_comm(arrays), lambda x, y, c: [(x, y, 1 - c)], name, SIBLING_SWAP_ID)[0]


def _sequencer_chips(send, name):
    return _sequencer_exchange(_chips_comm(send), lambda x, y, c: [(1 - x, y, c), (x, 1 - y, c), (1 - x, 1 - y, c)],
                               name, CHIP_EXCHANGE_ID)[0]


def _swap_comm(arrays):
    na = len(arrays)
    offs = np.concatenate([[0], np.cumsum([a.shape[1] for a in arrays])]).astype(int)

    def copies(ins, outs, sems):
        x, y, c = _mesh_pos()
        return [pltpu.make_async_remote_copy(
            src_ref=ins[ai].at[2 * k + 1 - c], dst_ref=outs[0].at[k, pl.ds(int(offs[ai]), arrays[ai].shape[1])],
            send_sem=sems[0].at[ai, k], recv_sem=sems[1].at[ai, k], device_id=(x, y, 1 - c), device_id_type=MESH)
            for ai in range(na) for k in range(4)]

    def start(ins, outs, sems):
        for cp in copies(ins, outs, sems):
            cp.start()

    def end(ins, outs, sems):
        for cp in copies(ins, outs, sems):
            cp.wait()

    return Comm(arrays, [jax.ShapeDtypeStruct((4, int(offs[-1]), PACK_COLS), arrays[0].dtype)],
                [pltpu.SemaphoreType.DMA((na, 4)), pltpu.SemaphoreType.DMA((na, 4))], start, end)


def _chips_comm(send):
    def copies(ins, outs, sems):
        x, y, c = _mesh_pos()
        chips = [(1 - x, y), (x, 1 - y), (1 - x, 1 - y)]
        return [pltpu.make_async_remote_copy(
            src_ref=ins[0].at[2 * cx + cy], dst_ref=outs[0].at[j], send_sem=sems[0].at[j], recv_sem=sems[1].at[j],
            device_id=(cx, cy, c), device_id_type=MESH) for j, (cx, cy) in enumerate(chips)]

    def start(ins, outs, sems):
        for cp in copies(ins, outs, sems):
            cp.start()

    def end(ins, outs, sems):
        for cp in copies(ins, outs, sems):
            cp.wait()

    return Comm([send], [jax.ShapeDtypeStruct((3,) + send.shape[1:], send.dtype)],
                [pltpu.SemaphoreType.DMA((3,)), pltpu.SemaphoreType.DMA((3,))], start, end)


def _pair_sum(keep, recv, name, tr=464):
    nchip, rows, cols = keep.shape

    def body(g_ref, r_ref, o_ref):
        o_ref[...] = (g_ref[...].astype(F32) + r_ref[...].astype(F32)).astype(BF16)

    blk = pl.BlockSpec((1, tr, cols), lambda k, i: (k, i, 0))
    return pl.pallas_call(
        body, name=name, grid=(nchip, rows // tr), in_specs=[blk, blk], out_specs=blk,
        out_shape=jax.ShapeDtypeStruct((nchip, rows, cols), BF16),
        compiler_params=_params(("parallel", "parallel")))(keep, recv)


def _pair_sum_pieces(pieces, recv, name, tr):
    _, rows, cols = pieces.shape
    core = lax.axis_index("c").astype(jnp.int32).reshape(1)

    def body(c_ref, g_ref, r_ref, o_ref):
        del c_ref
        o_ref[...] = (g_ref[...].astype(F32) + r_ref[...].astype(F32)).astype(BF16)

    grid_spec = pltpu.PrefetchScalarGridSpec(
        num_scalar_prefetch=1, grid=(4, rows // tr),
        in_specs=[pl.BlockSpec((1, tr, cols), lambda k, i, c_ref: (2 * k + c_ref[0], i, 0)),
                  pl.BlockSpec((1, tr, cols), lambda k, i, c_ref: (k, i, 0))],
        out_specs=pl.BlockSpec((1, tr, cols), lambda k, i, c_ref: (k, i, 0)))
    return pl.pallas_call(
        body, name=name, grid_spec=grid_spec, out_shape=jax.ShapeDtypeStruct((4, rows, cols), BF16),
        compiler_params=_params(("parallel", "parallel")))(core, pieces, recv)


def _chip_sum(own, others, name, tr=464):
    _, rows, cols = own.shape
    chip = (2 * lax.axis_index("x") + lax.axis_index("y")).astype(jnp.int32).reshape(1)

    def body(chip_ref, own_ref, oth_ref, o_ref):
        del chip_ref
        acc = own_ref[0].astype(F32)
        for j in range(3):
            acc = acc + oth_ref[j].astype(F32)
        o_ref[...] = acc

    grid_spec = pltpu.PrefetchScalarGridSpec(
        num_scalar_prefetch=1, grid=(rows // tr,),
        in_specs=[pl.BlockSpec((1, tr, cols), lambda i, chip_ref: (chip_ref[0], i, 0)),
                  pl.BlockSpec((3, tr, cols), lambda i, chip_ref: (0, i, 0))],
        out_specs=pl.BlockSpec((tr, cols), lambda i, chip_ref: (i, 0)))
    return pl.pallas_call(
        body, name=name, grid_spec=grid_spec, out_shape=jax.ShapeDtypeStruct((rows, cols), F32),
        compiler_params=_params(("parallel",)))(chip, own, others)


def _sum_leading(parts, name, tr=464):
    nparts, rows, cols = parts.shape
    tr = tr if rows % tr == 0 else rows

    def body(p_ref, o_ref):
        acc = p_ref[0].astype(F32)
        for i in range(1, nparts):
            acc = acc + p_ref[i].astype(F32)
        o_ref[...] = acc

    return pl.pallas_call(
        body, name=name, grid=(rows // tr,),
        in_specs=[pl.BlockSpec((nparts, tr, cols), lambda i: (0, i, 0))],
        out_specs=pl.BlockSpec((tr, cols), lambda i: (i, 0)), out_shape=jax.ShapeDtypeStruct((rows, cols), F32),
        compiler_params=_params(("parallel",)))(parts)


def _adamw(w, g, m, v, name, comm=None):
    shape = w.shape
    cols = shape[-1]
    lead = shape[0] if len(shape) >= 3 else 1
    rows = int(np.prod(shape[:-1])) // lead if len(shape) > 1 else 1
    w2, g2, m2, v2 = (a.reshape(lead, rows, cols) for a in (w, g, m, v))
    tr = max([t for t in range(8, min(rows, 512) + 1, 8) if rows % t == 0], default=rows)
    bc1, bc2 = 1.0 - ADAM_B1 ** ADAM_STEP, 1.0 - ADAM_B2 ** ADAM_STEP

    def body(w_ref, g_ref, m_ref, v_ref, d_ref, nm_ref, nv_ref):
        gv = g_ref[...]
        nm = ADAM_B1 * m_ref[...] + (1.0 - ADAM_B1) * gv
        nv = ADAM_B2 * v_ref[...] + (1.0 - ADAM_B2) * (gv * gv)
        nm_ref[...] = nm
        nv_ref[...] = nv
        d_ref[...] = -ADAM_LR * ((nm / bc1) / (jnp.sqrt(nv / bc2) + ADAM_EPS) + ADAM_WD * w_ref[...])

    blk = pl.BlockSpec((1, tr, cols), lambda l, i: (l, i, 0))
    res = _call(body, name=name, grid=(lead, rows // tr), in_specs=[blk] * 4, out_specs=[blk] * 3,
                out_shape=[jax.ShapeDtypeStruct((lead, rows, cols), F32)] * 3, sem=("parallel", "parallel"),
                args=(w2, g2, m2, v2), comm=comm)
    outs, couts = res if comm is not None else (res, None)
    outs = tuple(o.reshape(shape) for o in outs)
    return outs if comm is None else (outs, couts)


WEIGHT_NAMES = ['norm_mix_g', 'norm_xa_g', 'norm_ffn_g', 'norm_mem_g', 'norm_final_g', 'w_in_ab', 'conv_qkv_a',
                'a_log_a', 'dt_bias_a', 'onorm_g_a', 'ssm_lambda_re', 'ssm_lambda_im', 'ssm_b_re', 'ssm_b_im',
                'ssm_c_re', 'ssm_c_im', 'ssm_d', 'ssm_log_dt', 'w_glu_b', 'b_glu_b', 'w_out_ab', 'pool_w',
                'pool_scale', 'xa_wq', 'xa_wkv', 'xa_wo', 'ffn_w_up', 'ffn_conv', 'ffn_w_down']
BIG_SHARDED = {'w_in_ab': ((1, 1024, 2568), 2), 'w_glu_b': ((1, 512, 512), 1), 'w_out_ab': ((1, 1024, 1024), 1),
               'pool_w': ((1, 4, 256, 256), 2), 'xa_wq': ((2, 1024, 1024), 1), 'xa_wkv': ((2, 1024, 2048), 2),
               'xa_wo': ((2, 1024, 1024), 1), 'ffn_w_up': ((2, 1024, 5632), 2), 'ffn_w_down': ((2, 2816, 1024), 1)}
SMALL_SHARDED = {'conv_qkv_a': ((1, 4, 1536), 2), 'pool_scale': ((1, 1024), 1), 'ffn_conv': ((2, 3, 5632), 2)}
REPLICATED = {'norm_mix_g': (2, 1024), 'norm_xa_g': (2, 1024), 'norm_ffn_g': (2, 1024), 'norm_mem_g': (1024,),
              'norm_final_g': (1024,), 'a_log_a': (1, 4), 'dt_bias_a': (1, 4), 'onorm_g_a': (1, 128),
              'ssm_lambda_re': (1, 32, 64), 'ssm_lambda_im': (1, 32, 64), 'ssm_b_re': (1, 32, 64, 16),
              'ssm_b_im': (1, 32, 64, 16), 'ssm_c_re': (1, 32, 16, 64), 'ssm_c_im': (1, 32, 16, 64),
              'ssm_d': (1, 32, 16), 'ssm_log_dt': (1, 32), 'b_glu_b': (1, 512)}
PACK_ROW_ALIGN = 8


def _shard_shape(shape, axis):
    return tuple(s // N_DEV if i == axis else s for i, s in enumerate(shape))


def _round_up(n, m):
    return (n + m - 1) // m * m


def _pack(arrays):
    total = sum(int(np.prod(a.shape)) for a in arrays)
    padded = _round_up(total, PACK_COLS * PACK_ROW_ALIGN)
    parts = [a.astype(F32).reshape(-1) for a in arrays]
    if padded != total:
        parts.append(jnp.zeros((padded - total,), F32))
    return jnp.concatenate(parts).reshape(padded // PACK_COLS, PACK_COLS)


def _unpack(packed, shapes):
    flat, out, off = packed.reshape(-1), [], 0
    for shape in shapes:
        size = int(np.prod(shape))
        out.append(flat[off:off + size].reshape(shape))
        off += size
    return out


def _split_shards(full, axis):
    shape = full.shape
    s = shape[axis] // N_DEV
    a = full.reshape(shape[:axis] + (N_DEV, s) + shape[axis + 1:])
    return jnp.moveaxis(a, axis, 0).reshape(N_DEV, -1)


def _merge_shards(pieces, shape, axis):
    sh = _shard_shape(shape, axis)
    a = pieces.reshape((N_DEV,) + sh)
    a = jnp.moveaxis(a, 0, axis)
    return a.reshape(shape)


_SCAN_NB = SSM_CH // SCAN_CB


def _to_scan_layout(m, axis):
    shape = m.shape
    m = m.reshape(shape[:axis] + (2, _SCAN_NB, SCAN_CB) + shape[axis + 1:])
    return jnp.swapaxes(m, axis, axis + 1).reshape(shape)


def _from_scan_layout(m, axis):
    shape = m.shape
    m = m.reshape(shape[:axis] + (_SCAN_NB, 2, SCAN_CB) + shape[axis + 1:])
    return jnp.swapaxes(m, axis, axis + 1).reshape(shape)


def _s5_discretise(lam_re, lam_im, b_re, b_im, log_dt):
    dt = jnp.exp(log_dt)[:, None]
    mag = jnp.exp(lam_re * dt)
    ang = lam_im * dt
    lb_re, lb_im = mag * jnp.cos(ang), mag * jnp.sin(ang)
    den = lam_re * lam_re + lam_im * lam_im
    nr, ni = lb_re - 1.0, lb_im
    coef_re = (nr * lam_re + ni * lam_im) / den
    coef_im = (ni * lam_re - nr * lam_im) / den
    bb_re = coef_re[..., None] * b_re - coef_im[..., None] * b_im
    bb_im = coef_re[..., None] * b_im + coef_im[..., None] * b_re
    return lb_re, lb_im, bb_re, bb_im


_GROUPS_PER_BLOCK = N_GROUPS // _SCAN_NB
_U_BLOCK = _GROUPS_PER_BLOCK * SSM_GROUP


def _s5_matrices(lb_re, lb_im, bb_re, bb_im, c_re, c_im):
    eye = jnp.eye(_GROUPS_PER_BLOCK, dtype=F32)
    blocked = lambda m: m.reshape((_SCAN_NB, _GROUPS_PER_BLOCK) + m.shape[1:])
    bmat = lambda bb: jnp.einsum('jgph,gk->jghkp', blocked(bb), eye).reshape(_SCAN_NB, _U_BLOCK, SCAN_CB)
    cmat = lambda cc: jnp.einsum('jghp,gk->jkpgh', blocked(cc), eye).reshape(_SCAN_NB, SCAN_CB, _U_BLOCK)
    b_in = jnp.concatenate([bmat(bb_re), bmat(bb_im)], axis=2)
    c_out = jnp.concatenate([cmat(c_re), -cmat(c_im)], axis=1)
    a_row = _to_scan_layout(jnp.concatenate([lb_re.reshape(1, SSM_CH), lb_im.reshape(1, SSM_CH)], axis=1), 1)
    return b_in, c_out, a_row


def _s5_matrix_grads(db_in, dc_out, da_row):
    da_nat = _from_scan_layout(da_row, 1)
    eye = jnp.eye(_GROUPS_PER_BLOCK, dtype=F32)
    nb, gb = _SCAN_NB, _GROUPS_PER_BLOCK
    bgrad = lambda m: jnp.einsum('jghkp,gk->jgph', m.reshape(nb, gb, SSM_GROUP, gb, SSM_STATE), eye
                                 ).reshape(N_GROUPS, SSM_STATE, SSM_GROUP)
    cgrad = lambda m: jnp.einsum('jkpgh,gk->jghp', m.reshape(nb, gb, SSM_STATE, gb, SSM_GROUP), eye
                                 ).reshape(N_GROUPS, SSM_GROUP, SSM_STATE)
    dbb_re, dbb_im = bgrad(db_in[:, :, :SCAN_CB]), bgrad(db_in[:, :, SCAN_CB:])
    dc_re, dc_im = cgrad(dc_out[:, :SCAN_CB]), -cgrad(dc_out[:, SCAN_CB:])
    dlb_re = da_nat[0, :SSM_CH].reshape(N_GROUPS, SSM_STATE)
    dlb_im = da_nat[0, SSM_CH:].reshape(N_GROUPS, SSM_STATE)
    return dlb_re, dlb_im, dbb_re, dbb_im, dc_re, dc_im


def _as_pieces(a):
    return a.reshape(N_DEV, a.shape[0] // N_DEV, a.shape[1])


def _hybrid_fwd(xn, x, wts, p, weights, riders):
    sv = {}
    hq = _mm(xn, wts['w_qkv_t'], "nt", "l0_in_qkv")
    gate = _mm(xn, wts['w_gate_t'], "nt", "l0_in_gate")
    ba = _mm(xn, wts['w_ba_t'], "nt", "l0_in_ba")
    u = _mm(xn, wts['w_u_t'], "nt", "l0_in_u")
    conv = p['conv_qkv']
    qkv = _qkv_pre_fwd(hq, conv, "l0_qkv_pre")
    gates = _gates_fwd(ba, p['arow'], p['brow'], "l0_gates")
    o, tm_all, s_all = riders.run("l0_gdr_fwd", _gdr_fwd, qkv, gates)
    wts['w_glu'], wts['w_out'] = weights.full['w_glu'], weights.full['w_out']
    y_a = _onorm_fwd(o, gate, p['onorm_g'], "l0_onorm")
    bu = riders.run("l0_s5_bu", _mm_bd, u, p['b_in'], "nn")
    xs = riders.run("l0_s5_scan", _s5_scan_fwd, bu, p['a_row'])
    weights.gather_by_sequencer(GATHER_LAYER1, xs, "gather_layer1", GATHER_LAYER1_ID)
    yc = riders.run("l0_s5_cx", _mm_bd, xs, p['c_out'], "nn")
    yl, y_b = _glu_fwd(yc, u, p['d_row'], wts['w_glu'], p['b_glu'], "l0_glu")
    mixed = jnp.concatenate([y_a, y_b], axis=1)
    x1 = _mm(mixed, wts['w_out'], "nn", "l0_out", res=x)
    sv.update(hq=hq, gate=gate, ba=ba, u=u, qkv=qkv, gb=gates, o=o, tm=tm_all, s=s_all, xs=xs, yl=yl, mixed=mixed)
    return x1, sv


def _hybrid_bwd(dx1, xn, wts, p, sv, riders):
    gr = {}
    dmixed = _mm(dx1, wts['w_out'], "nt", "l0_out_dx", out_dtype=BF16)
    riders.grad('w_out', _as_pieces(_mm(sv['mixed'], dx1, "tn", "l0_out_dw", out_dtype=BF16)))
    dya, dyb = dmixed[:, :WIDTH_A], dmixed[:, WIDTH_A:]
    dyl, du_direct, dw_glu, gr['b_glu_b'], dd = _glu_bwd(
        sv['yl'], sv['u'], p['d_row'], wts['w_glu'], p['b_glu'], dyb, "l0_glu_bwd")
    riders.grad('w_glu', dw_glu.astype(BF16).reshape(N_DEV, -1, PACK_COLS))
    dxs = riders.run("l0_s5_cx_dx", _mm_bd, dyl, p['c_out'], "nt")
    dc_out = _mm_bd(sv['xs'], dyl, "tn", "l0_s5_cx_dw")
    lam, da_row = riders.run("l0_s5_scan_bwd", _s5_scan_bwd, dxs, sv['xs'], p['a_row'])
    du = _mm_bd(lam, p['b_in'], "nt", "l0_s5_bu_dx", res=du_direct, out_dtype=BF16)
    db_in = _mm_bd(sv['u'], lam, "tn", "l0_s5_bu_dw")
    gr['s5'] = (db_in, dc_out, da_row, dd)
    do, dgate, gr['onorm_g_a'] = _onorm_bwd(sv['o'], sv['gate'], p['onorm_g'], dya, "l0_onorm_bwd")
    dqkv, dgb = riders.run("l0_gdr_bwd", _gdr_bwd, sv['qkv'], sv['gb'], sv['tm'], sv['s'], do)
    dhq, gr['conv_qkv_a'] = _qkv_pre_bwd(sv['hq'], p['conv_qkv'], dqkv, "l0_qkv_pre_bwd")
    dba, da_log, ddt_bias = _gates_bwd(sv['ba'], p['arow'], p['brow'], dgb, "l0_gates_bwd")
    gr['a_log_a'], gr['dt_bias_a'] = da_log[:, 4:8], ddt_bias[:, 4:8]
    dw_qkv_t = _mm(dhq, xn, "tn", "l0_in_qkv_dw", out_dtype=BF16)
    dw_gate_t = _mm(dgate, xn, "tn", "l0_in_gate_dw", out_dtype=BF16)
    dw_ba_t = _mm(dba, xn, "tn", "l0_in_ba_dw", out_dtype=BF16)
    dw_u_t = _mm(du, xn, "tn", "l0_in_u_dw", out_dtype=BF16)
    dw_in_t = _as_pieces(jnp.concatenate([dw_qkv_t, dw_gate_t, dw_ba_t[:8], dw_u_t], axis=0))
    riders.grad('w_in_t', jnp.concatenate(
        [dw_in_t, jnp.zeros((N_DEV, dict(PIECES)['w_in_t'] - W_IN_PIECE, D_MODEL), BF16)], axis=1))
    dxn = riders.run("l0_in_qkv_dx", _mm, dhq, wts['w_qkv_t'], "nn")
    dxn = _mm(dgate, wts['w_gate_t'], "nn", "l0_in_gate_dx", res=dxn)
    dxn = _mm(dba, wts['w_ba_t'], "nn", "l0_in_ba_dx", res=dxn)
    dxn = riders.run("l0_in_u_dx", _mm, du, wts['w_u_t'], "nn", res=dxn)
    return dxn, gr


def _xa_fwd(x1, g, mem_n, wq, wkv_t, wo, tag, riders):
    xq = _rms_fwd(x1, g, BF16, tag + "_norm")
    q = _mm(xq, wq, "nn", tag + "_q", out_dtype=BF16)
    kv = _mm(mem_n, wkv_t, "nt", tag + "_kv", out_dtype=BF16)
    o = riders.run(tag + "_attn", _attn_fwd, q, kv)
    x2 = _mm(o, wo, "nn", tag + "_o", res=x1)
    return x2, dict(xq=xq, q=q, kv=kv, o=o)


def _xa_bwd(dx2, x1, g, mem_n, wq, wkv_t, wo, sv, tag, layer, riders):
    do = _mm(dx2, wo, "nt", tag + "_o_dx", out_dtype=BF16)
    riders.grad('wo%d' % layer, _as_pieces(_mm(sv['o'], dx2, "tn", tag + "_o_dw", out_dtype=BF16)))
    dq, dk, dv = _attn_bwd(sv['q'], sv['kv'], do, tag + "_attn_bwd")
    dkv = jnp.concatenate([dk, dv], axis=1).astype(BF16)
    dxq = _mm(dq, wq, "nt", tag + "_q_dx")
    riders.grad('wq%d' % layer, _as_pieces(_mm(sv['xq'], dq, "tn", tag + "_q_dw", out_dtype=BF16)))
    dmem_n = _mm(dkv, wkv_t, "nn", tag + "_kv_dx")
    riders.grad('wkv_t%d' % layer, _as_pieces(_mm(dkv, mem_n, "tn", tag + "_kv_dw", out_dtype=BF16)))
    dx1, dg = riders.run(tag + "_norm_bwd", _rms_bwd, x1, g, dxq, dx2)
    return dx1, dmem_n, dg


def _ffn_fwd(x2, g, w_up_t, conv, w_down, tag, riders):
    xf = _rms_fwd(x2, g, BF16, tag + "_norm")
    h = riders.run(tag + "_up", _mm, xf, w_up_t, "nt")
    a = riders.run(tag + "_act", _ffn_act_fwd, h, conv)
    x3 = _mm(a, w_down, "nn", tag + "_down", res=x2)
    return x3, dict(xf=xf, h=h, a=a)


def _ffn_bwd(dx3, x2, g, w_up_t, conv, w_down, sv, tag, layer, riders):
    da = _mm(dx3, w_down, "nt", tag + "_down_dx")
    riders.grad('down%d' % layer, _as_pieces(_mm(sv['a'], dx3, "tn", tag + "_down_dw", out_dtype=BF16)))
    dh, dconv = riders.run(tag + "_act_bwd", _ffn_act_bwd, sv['h'], conv, da)
    dxf = riders.run(tag + "_up_dx", _mm_parts, dh, w_up_t, "nn")
    dw_up_t = riders.run(tag + "_up_dw", _mm_parts, dh, sv['xf'], "tn", out_dtype=BF16)
    riders.grad('up_t%d' % layer, _as_pieces(dw_up_t))
    dx2, dg = riders.run(tag + "_norm_bwd", _rms_bwd, x2, g, dxf, dx3)
    return dx2, dconv, dg


BIG_NAMES, SMALL_NAMES, REP_NAMES = list(BIG_SHARDED), list(SMALL_SHARDED), list(REPLICATED)
SMALL_SIZES = [int(np.prod(_shard_shape(*SMALL_SHARDED[n]))) for n in SMALL_NAMES]


PIECES = [('w_in_t', 384), ('w_glu', 32), ('w_out', 128), ('pool_w', 32), ('wq0', 128), ('wq1', 128),
          ('wkv_t0', 256), ('wkv_t1', 256), ('wo0', 128), ('wo1', 128), ('up_t0', 704), ('up_t1', 704),
          ('down0', 352), ('down1', 352)]
W_IN_ROWS = 4 * WIDTH_A + 2 * N_HEADS_A + SSM_WIDTH
W_IN_PIECE = W_IN_ROWS // N_DEV


def _row_tile(rows):
    return max(t for t in range(16, min(rows, 1024) + 1, 16) if rows % t == 0)


class _Riders:
    def __init__(self):
        self.waiting = {}
        self.deferred = {}
        self.grads = {}
        self.groups = []
        self.reduced = {}

    def add(self, host, comm, then):
        self.waiting.setdefault(host, []).append((comm, then))

    def after(self, marker, then):
        self.deferred.setdefault(marker, []).append(then)

    def mark(self, name, out=None):
        for cont in self.deferred.pop(name, []):
            step = cont()
            if step is not None:
                values, then = step
                out, values = lax.optimization_barrier((out, values))
                then(values)
        return out

    def run(self, name, fn, *args, **kw):
        riders = self.waiting.pop(name, [])
        if not riders:
            out = fn(*args, name=name, **kw)
        else:
            out, couts = fn(*args, name=name, comm=[c for c, _ in riders], **kw)
            for (_, then), got in zip(riders, couts):
                then(got)
        return self.mark(name, out)

    def grad(self, key, pieces):
        self.grads[key] = pieces
        for group in [g for g in self.groups if all(k in self.grads for k in g[1])]:
            self.groups.remove(group)
            self._reduce(*group)

    def _reduce(self, name, keys, pair_marker, sum_marker):
        arrays = [self.grads[k] for k in keys]
        rows = sum(a.shape[1] for a in arrays)
        tile = _row_tile(rows)
        from_sibling = _sequencer_swap(arrays, name + "_to_sibling")

        def after_swap():
            if len(arrays) == 1:
                chip_sums = _pair_sum_pieces(arrays[0], from_sibling, name + "_pair_sum", tr=tile)
            else:
                core = lax.axis_index("c")
                keep = jnp.concatenate(
                    [lax.dynamic_index_in_dim(a.reshape(4, 2, a.shape[1], PACK_COLS), core, 1, keepdims=False)
                     for a in arrays], axis=1)
                chip_sums = _pair_sum(keep, from_sibling, name + "_pair_sum", tr=tile)

            def exchange_among_chips(chip_sums):
                from_chips = _sequencer_chips(chip_sums, name + "_to_chips")

                def store(total):
                    off = 0
                    for k, a in zip(keys, arrays):
                        self.reduced[k] = total[off:off + a.shape[1]]
                        off += a.shape[1]

                self.after(sum_marker, lambda: (
                    _chip_sum(chip_sums, from_chips, name + "_chip_sum", tr=tile), store))

            return chip_sums, exchange_among_chips

        self.after(pair_marker, after_swap)


class _Weights:
    def __init__(self, inp):
        bf = lambda a: a.astype(BF16)
        local = {'w_in_t': bf(inp['w_in_ab'][0]).T, 'w_glu': bf(inp['w_glu_b'][0]), 'w_out': bf(inp['w_out_ab'][0]),
                 'pool_w': bf(inp['pool_w'][0]),
                 'small': _pack([inp[n] for n in SMALL_NAMES])}
        for l in range(2):
            local['wq%d' % l] = bf(inp['xa_wq'][l])
            local['wkv_t%d' % l] = bf(inp['xa_wkv'][l]).T
            local['wo%d' % l] = bf(inp['xa_wo'][l])
            local['up_t%d' % l] = bf(inp['ffn_w_up'][l]).T
            local['down%d' % l] = bf(inp['ffn_w_down'][l])
        self.local, self.full = local, {}

    def plan(self, keys):
        return _gather_comm([self.local[k] for k in keys])

    def gather_by_sequencer(self, keys, after, name, collective_id):
        arrays = [self.local[k] for k in keys]
        tie = (after.reshape(-1)[0] * 0.0).astype(arrays[0].dtype)
        arrays[0] = arrays[0] + tie
        self.land(keys, _sequencer_gather(arrays, name, collective_id))

    def land(self, keys, gathered):
        for k, g in zip(keys, gathered):
            if k == 'small':
                off = 0
                for n, size in zip(SMALL_NAMES, SMALL_SIZES):
                    self.full[n] = _merge_shards(g.reshape(N_DEV, -1)[:, off:off + size], *SMALL_SHARDED[n])
                    off += size
            elif k == 'pool_w':
                self.full[k] = jnp.swapaxes(g, 0, 1).reshape(len(POOL_WINDOWS), POOL_GROUP, POOL_GROUP)
            else:
                self.full[k] = g.reshape(N_DEV * g.shape[1], g.shape[2])


GATHER_FIRST = ['w_in_t', 'small']
GATHER_LAYER0 = ['w_glu', 'w_out', 'wq0', 'wkv_t0', 'wo0', 'down0', 'up_t0']
GATHER_LAYER1 = ['pool_w', 'wq1', 'wkv_t1', 'wo1', 'up_t1', 'down1']
GATHER_LAYER0_ID, GATHER_LAYER1_ID = 7, 8
GRAD_RIDES = [('g_down1', ['down1'], 'l1_ffn_up_dx', 'l1_xa_norm_bwd'),
              ('g_up1', ['up_t1'], 'l1_xa_norm_bwd', 'l0_ffn_up_dx'),
              ('g_xa1', ['wq1', 'wkv_t1', 'wo1', 'pool_w'], 'l0_ffn_up_dx', 'l0_xa_norm_bwd'),
              ('g_down0', ['down0'], 'l0_ffn_up_dx', 'l0_s5_scan_bwd'),
              ('g_l0', ['up_t0', 'wq0', 'wkv_t0', 'wo0'], 'l0_s5_cx_dx', 'l0_in_u_dx'),
              ('g_out', ['w_out', 'w_glu'], 'l0_s5_scan_bwd', 'l0_in_u_dx'),
              ('g_in', ['w_in_t'], 'l0_in_u_dx', 'adamw_pool_w')]


def _local_step(inp):
    f32_of = lambda n: inp[n].astype(F32)
    weights = _Weights(inp)
    riders = _Riders()
    riders.groups = list(GRAD_RIDES)
    full = weights.full
    weights.land(GATHER_FIRST, _comm_only(weights.plan(GATHER_FIRST), "gather_first"))
    weights.gather_by_sequencer(GATHER_LAYER0, full['w_in_t'], "gather_layer0", GATHER_LAYER0_ID)
    w_in_t = full['w_in_t']
    wts0 = dict(w_qkv_t=w_in_t[:3 * WIDTH_A], w_gate_t=w_in_t[3 * WIDTH_A:4 * WIDTH_A],
                w_ba_t=jnp.concatenate([w_in_t[4 * WIDTH_A:4 * WIDTH_A + 8], jnp.zeros((LANE - 8, D_MODEL), BF16)], 0),
                w_u_t=w_in_t[4 * WIDTH_A + 8:])
    lb_disc, disc_vjp = jax.vjp(_s5_discretise, f32_of('ssm_lambda_re')[0], f32_of('ssm_lambda_im')[0],
                                f32_of('ssm_b_re')[0], f32_of('ssm_b_im')[0], f32_of('ssm_log_dt')[0])
    b_in, c_out, a_row = _s5_matrices(*lb_disc, f32_of('ssm_c_re')[0], f32_of('ssm_c_im')[0])
    zeros4 = jnp.zeros((1, 4), F32)
    p0 = dict(conv_qkv=full['conv_qkv_a'][0], onorm_g=f32_of('onorm_g_a'),
              arow=jnp.concatenate([zeros4, f32_of('a_log_a'), jnp.zeros((1, LANE - 8), F32)], 1),
              brow=jnp.concatenate([zeros4, f32_of('dt_bias_a'), jnp.zeros((1, LANE - 8), F32)], 1),
              b_in=b_in.astype(BF16), c_out=c_out.astype(BF16), a_row=a_row,
              d_row=f32_of('ssm_d').reshape(1, SSM_WIDTH), b_glu=f32_of('b_glu_b'))

    x0 = inp['x'][0]
    mem_n = _rms_fwd(inp['mem'][0], inp['norm_mem_g'], BF16, "mem_norm")
    xn0 = _rms_fwd(x0, inp['norm_mix_g'][0], BF16, "l0_mix_norm")
    x1, sv_mix0 = _hybrid_fwd(xn0, x0, wts0, p0, weights, riders)
    x2, sv_xa0 = _xa_fwd(x1, inp['norm_xa_g'][0], mem_n, full['wq0'], full['wkv_t0'], full['wo0'], "l0_xa", riders)
    x3, sv_ffn0 = _ffn_fwd(x2, inp['norm_ffn_g'][0], full['up_t0'], full['ffn_conv'][0], full['down0'], "l0_ffn", riders)
    xn1 = _rms_fwd(x3, inp['norm_mix_g'][1], F32, "l1_mix_norm")
    x4 = _pool_fwd(xn1, full['pool_w'], full['pool_scale'], x3, "l1_pool")
    x5, sv_xa1 = _xa_fwd(x4, inp['norm_xa_g'][1], mem_n, full['wq1'], full['wkv_t1'], full['wo1'], "l1_xa", riders)
    x6, sv_ffn1 = _ffn_fwd(x5, inp['norm_ffn_g'][1], full['up_t1'], full['ffn_conv'][1], full['down1'], "l1_ffn", riders)
    loss_part, dx6, dg_final = _loss_head(x6, inp['norm_final_g'], inp['loss_target'][0], "loss_head")

    dx5, dconv1, dg_ffn1 = _ffn_bwd(dx6, x5, inp['norm_ffn_g'][1], full['up_t1'], full['ffn_conv'][1], full['down1'],
                                    sv_ffn1, "l1_ffn", 1, riders)
    dx4, dmem1, dg_xa1 = _xa_bwd(dx5, x4, inp['norm_xa_g'][1], mem_n, full['wq1'], full['wkv_t1'], full['wo1'],
                                 sv_xa1, "l1_xa", 1, riders)
    dxn1, dpool_w, dpool_scale = _pool_bwd(xn1, full['pool_w'], full['pool_scale'], dx4, "l1_pool_bwd")
    pool_pieces = jnp.swapaxes(dpool_w.astype(BF16).reshape(len(POOL_WINDOWS), N_DEV, -1, POOL_GROUP), 0, 1)
    riders.grad('pool_w', pool_pieces.reshape(N_DEV, -1, PACK_COLS))
    dx3, dg_mix1 = riders.run("l1_mix_norm_bwd", _rms_bwd, x3, inp['norm_mix_g'][1], dxn1, dx4)
    dx2, dconv0, dg_ffn0 = _ffn_bwd(dx3, x2, inp['norm_ffn_g'][0], full['up_t0'], full['ffn_conv'][0], full['down0'],
                                    sv_ffn0, "l0_ffn", 0, riders)
    dx1, dmem0, dg_xa0 = _xa_bwd(dx2, x1, inp['norm_xa_g'][0], mem_n, full['wq0'], full['wkv_t0'], full['wo0'],
                                 sv_xa0, "l0_xa", 0, riders)
    dxn0, g_mix0 = _hybrid_bwd(dx1, xn0, wts0, p0, sv_mix0, riders)
    grad_x, dg_mix0 = _rms_bwd(x0, inp['norm_mix_g'][0], dxn0, dx1, "l0_mix_norm_bwd")
    _, dg_mem = _rms_bwd(inp['mem'][0], inp['norm_mem_g'], dmem0 + dmem1, None, "mem_norm_bwd")
    assert not riders.groups and not riders.waiting and all(k.startswith("adamw_") for k in riders.deferred), (
        riders.groups, list(riders.waiting), list(riders.deferred))

    db_in, dc_out, da_row, dd = g_mix0['s5']
    dlb_re, dlb_im, dbb_re, dbb_im, dc_re, dc_im = _s5_matrix_grads(db_in, dc_out, da_row)
    dlam_re, dlam_im, dbr, dbi, dlog_dt = disc_vjp((dlb_re, dlb_im, dbb_re, dbb_im))

    rep_grads = {
        'norm_mix_g': jnp.concatenate([dg_mix0, dg_mix1], 0), 'norm_xa_g': jnp.concatenate([dg_xa0, dg_xa1], 0),
        'norm_ffn_g': jnp.concatenate([dg_ffn0, dg_ffn1], 0), 'norm_mem_g': dg_mem.reshape(-1),
        'norm_final_g': dg_final.reshape(-1), 'a_log_a': g_mix0['a_log_a'], 'dt_bias_a': g_mix0['dt_bias_a'],
        'onorm_g_a': g_mix0['onorm_g_a'], 'ssm_lambda_re': dlam_re[None], 'ssm_lambda_im': dlam_im[None],
        'ssm_b_re': dbr[None], 'ssm_b_im': dbi[None], 'ssm_c_re': dc_re[None], 'ssm_c_im': dc_im[None],
        'ssm_d': dd.reshape(1, N_GROUPS, SSM_GROUP), 'ssm_log_dt': dlog_dt[None], 'b_glu_b': g_mix0['b_glu_b']}
    small_grads = {'conv_qkv_a': g_mix0['conv_qkv_a'][None], 'pool_scale': dpool_scale,
                   'ffn_conv': jnp.stack([dconv0, dconv1])}
    return loss_part, grad_x, riders, rep_grads, small_grads


ADAMW_ORDER = ['ffn_w_up', 'ffn_w_down', 'xa_wkv', 'xa_wq', 'xa_wo', 'w_out_ab', 'w_glu_b', 'pool_w', 'w_in_ab']


def _update(inp, loss_part, grad_x, riders, rep_grads, small_grads):
    dev = _device_index()
    misc_local = _pack([rep_grads[n] for n in REP_NAMES] + [small_grads[n] for n in SMALL_NAMES] + [loss_part])
    (misc_all,) = _sequencer_gather([misc_local], "gather_small_grads", GATHER_LAYER0_ID)
    piece = lambda key: riders.reduced[key]
    both = lambda name: jnp.stack([piece(name + '0'), piece(name + '1')])
    swap = lambda a: jnp.swapaxes(a, -1, -2)
    reduced = {'w_in_ab': lambda: piece('w_in_t')[:W_IN_PIECE][None],
               'w_glu_b': lambda: piece('w_glu').reshape(inp['w_glu_b'].shape),
               'w_out_ab': lambda: piece('w_out')[None], 'pool_w': lambda: piece('pool_w').reshape(inp['pool_w'].shape),
               'xa_wq': lambda: both('wq'), 'xa_wkv': lambda: both('wkv_t'), 'xa_wo': lambda: both('wo'),
               'ffn_w_up': lambda: both('up_t'), 'ffn_w_down': lambda: both('down')}
    transposed = ('w_in_ab', 'xa_wkv', 'ffn_w_up')
    grads, upd = {}, {}
    assert sorted(ADAMW_ORDER) == sorted(BIG_NAMES)
    for n in ADAMW_ORDER:
        fix = swap if n in transposed else (lambda a: a)
        g = reduced[n]()
        out = riders.run("adamw_" + n, _adamw, fix(inp[n]), g, fix(inp['m_' + n]), fix(inp['v_' + n]))
        upd[n], grads[n] = tuple(fix(o) for o in out), fix(g)
    assert not riders.waiting and not riders.deferred, (list(riders.waiting), list(riders.deferred))
    misc_sum = _sum_leading(misc_all, "small_grads_sum")
    misc = _unpack(misc_sum, [inp[n].shape for n in REP_NAMES] + [SMALL_SHARDED[n][0] for n in SMALL_NAMES] + [()])
    loss = misc.pop()
    for n, g in zip(REP_NAMES, misc):
        grads[n] = g
    for n, g in zip(SMALL_NAMES, misc[len(REP_NAMES):]):
        grads[n] = lax.dynamic_index_in_dim(_split_shards(g, SMALL_SHARDED[n][1]), dev, 0, keepdims=False
                                            ).reshape(inp[n].shape)
    tiny_names = REP_NAMES + SMALL_NAMES
    rep_total = sum(int(np.prod(inp[n].shape)) for n in REP_NAMES)
    packs = [_pack([inp[prefix + n] for n in tiny_names]) for prefix in ('', 'm_', 'v_')]
    g_pack = _pack([misc_sum.reshape(-1)[:rep_total]] + [grads[n] for n in SMALL_NAMES])
    tiny_out = [_unpack(o, [inp[n].shape for n in tiny_names])
                for o in _adamw(packs[0], g_pack, packs[1], packs[2], "adamw_small")]
    for i, n in enumerate(tiny_names):
        upd[n] = tuple(o[i] for o in tiny_out)

    outs = [loss, grad_x[None]]
    outs += [grads[n] for n in WEIGHT_NAMES]
    for i in range(3):
        outs += [upd[n][i] for n in WEIGHT_NAMES]
    return tuple(outs)


def _step(inp):
    loss_part, grad_x, riders, rep_grads, small_grads = _local_step(inp)
    return _update(inp, loss_part, grad_x, riders, rep_grads, small_grads)


INPUT_NAMES = (['x', 'mem'] + WEIGHT_NAMES + ['loss_target'] + ['m_' + n for n in WEIGHT_NAMES]
               + ['v_' + n for n in WEIGHT_NAMES])


def kernel(x, mem, norm_mix_g, norm_xa_g, norm_ffn_g, norm_mem_g, norm_final_g, w_in_ab, conv_qkv_a, a_log_a, dt_bias_a, onorm_g_a, ssm_lambda_re, ssm_lambda_im, ssm_b_re, ssm_b_im, ssm_c_re, ssm_c_im, ssm_d, ssm_log_dt, w_glu_b, b_glu_b, w_out_ab, pool_w, pool_scale, xa_wq, xa_wkv, xa_wo, ffn_w_up, ffn_conv, ffn_w_down, loss_target, m_norm_mix_g, m_norm_xa_g, m_norm_ffn_g, m_norm_mem_g, m_norm_final_g, m_w_in_ab, m_conv_qkv_a, m_a_log_a, m_dt_bias_a, m_onorm_g_a, m_ssm_lambda_re, m_ssm_lambda_im, m_ssm_b_re, m_ssm_b_im, m_ssm_c_re, m_ssm_c_im, m_ssm_d, m_ssm_log_dt, m_w_glu_b, m_b_glu_b, m_w_out_ab, m_pool_w, m_pool_scale, m_xa_wq, m_xa_wkv, m_xa_wo, m_ffn_w_up, m_ffn_conv, m_ffn_w_down, v_norm_mix_g, v_norm_xa_g, v_norm_ffn_g, v_norm_mem_g, v_norm_final_g, v_w_in_ab, v_conv_qkv_a, v_a_log_a, v_dt_bias_a, v_onorm_g_a, v_ssm_lambda_re, v_ssm_lambda_im, v_ssm_b_re, v_ssm_b_im, v_ssm_c_re, v_ssm_c_im, v_ssm_d, v_ssm_log_dt, v_w_glu_b, v_b_glu_b, v_w_out_ab, v_pool_w, v_pool_scale, v_xa_wq, v_xa_wkv, v_xa_wo, v_ffn_w_up, v_ffn_conv, v_ffn_w_down):
    args = (x, mem, norm_mix_g, norm_xa_g, norm_ffn_g, norm_mem_g, norm_final_g, w_in_ab, conv_qkv_a, a_log_a, dt_bias_a, onorm_g_a, ssm_lambda_re, ssm_lambda_im, ssm_b_re, ssm_b_im, ssm_c_re, ssm_c_im, ssm_d, ssm_log_dt, w_glu_b, b_glu_b, w_out_ab, pool_w, pool_scale, xa_wq, xa_wkv, xa_wo, ffn_w_up, ffn_conv, ffn_w_down, loss_target, m_norm_mix_g, m_norm_xa_g, m_norm_ffn_g, m_norm_mem_g, m_norm_final_g, m_w_in_ab, m_conv_qkv_a, m_a_log_a, m_dt_bias_a, m_onorm_g_a, m_ssm_lambda_re, m_ssm_lambda_im, m_ssm_b_re, m_ssm_b_im, m_ssm_c_re, m_ssm_c_im, m_ssm_d, m_ssm_log_dt, m_w_glu_b, m_b_glu_b, m_w_out_ab, m_pool_w, m_pool_scale, m_xa_wq, m_xa_wkv, m_xa_wo, m_ffn_w_up, m_ffn_conv, m_ffn_w_down, v_norm_mix_g, v_norm_xa_g, v_norm_ffn_g, v_norm_mem_g, v_norm_final_g, v_w_in_ab, v_conv_qkv_a, v_a_log_a, v_dt_bias_a, v_onorm_g_a, v_ssm_lambda_re, v_ssm_lambda_im, v_ssm_b_re, v_ssm_b_im, v_ssm_c_re, v_ssm_c_im, v_ssm_d, v_ssm_log_dt, v_w_glu_b, v_b_glu_b, v_w_out_ab, v_pool_w, v_pool_scale, v_xa_wq, v_xa_wkv, v_xa_wo, v_ffn_w_up, v_ffn_conv, v_ffn_w_down)
    return _step(dict(zip(INPUT_NAMES, args)))
```

```python
import functools
import math

import numpy as np
import jax
import jax.numpy as jnp
from jax import lax
from jax.experimental import pallas as pl
from jax.experimental.pallas import tpu as pltpu
from jax.experimental.pallas import tpu_sc as plsc

F32, BF16 = jnp.float32, jnp.bfloat16
HIGH, HIGHEST = lax.Precision.HIGH, lax.Precision.HIGHEST
MESH = pl.DeviceIdType.MESH

N_DEV = 8
SEQ, D_MODEL, MEM_LEN = 2048, 1024, 256
WIDTH_A, N_HEADS_A, HEAD_A, CONV_A = 512, 4, 128, 4
GDR_CHUNK = 128
SSM_WIDTH, SSM_GROUP, N_GROUPS, SSM_STATE = 512, 16, 32, 64
SSM_CH = N_GROUPS * SSM_STATE
SCAN_CB = 512
POOL_WINDOWS = (2, 4, 8, 16)
POOL_GROUP = 256
N_HEADS_X, HEAD_X = 4, 256
D_FF, CONV_FFN = 2816, 3
RMS_EPS = 1e-6
ADAM_LR, ADAM_B1, ADAM_B2, ADAM_EPS, ADAM_WD, ADAM_STEP = 0.001, 0.9, 0.999, 1e-08, 0.01, 10
LANE = 128
PACK_COLS = 1024
VMEM_LIMIT_BYTES = 56 * 1024 * 1024


def _params(sem=None):
    return pltpu.CompilerParams(dimension_semantics=sem, vmem_limit_bytes=VMEM_LIMIT_BYTES)


class Comm:
    def __init__(self, inputs, out_shapes, sems, start, end, mid=None):
        self.inputs, self.out_shapes, self.sems = list(inputs), list(out_shapes), list(sems)
        self.start, self.mid, self.end = start, mid, end


def _merge_comms(comms):
    comms = [c for c in comms if c is not None]
    if not comms:
        return None, []
    bounds, ni, no, ns = [], 0, 0, 0
    for c in comms:
        bounds.append((ni, no, ns))
        ni, no, ns = ni + len(c.inputs), no + len(c.out_shapes), ns + len(c.sems)

    def phase(which):
        def run(ins, outs, sems):
            for c, (i0, o0, s0) in zip(comms, bounds):
                fn = getattr(c, which)
                if fn is not None:
                    fn(ins[i0:i0 + len(c.inputs)], outs[o0:o0 + len(c.out_shapes)], sems[s0:s0 + len(c.sems)])
        return run

    merged = Comm([a for c in comms for a in c.inputs], [s for c in comms for s in c.out_shapes],
                  [s for c in comms for s in c.sems], phase("start"), phase("end"), phase("mid"))
    return merged, [(o0, o0 + len(c.out_shapes)) for c, (_, o0, _) in zip(comms, bounds)]


def _call(body, *, name, grid, in_specs, out_specs, out_shape, args, scratch_shapes=(), sem=None, comm=None):
    single = not isinstance(out_shape, (list, tuple))
    out_specs_l = [out_specs] if single else list(out_specs)
    out_shape_l = [out_shape] if single else list(out_shape)
    scratch_shapes = list(scratch_shapes)
    merged, spans = _merge_comms(comm if isinstance(comm, (list, tuple)) else [comm])
    if merged is None:
        outs = pl.pallas_call(body, name=name, grid=grid, in_specs=list(in_specs), out_specs=out_specs_l,
                              out_shape=out_shape_l, scratch_shapes=scratch_shapes, compiler_params=_params(sem))(*args)
        outs = outs[0] if single else outs
        return outs if comm is None else (outs, [])
    n_in, n_out, n_scr = len(in_specs), len(out_specs_l), len(scratch_shapes)
    ci, co = len(merged.inputs), len(merged.out_shapes)
    total = int(np.prod(grid))

    def wrapped(*refs):
        ins, cins = refs[:n_in], refs[n_in:n_in + ci]
        outs, couts = refs[n_in + ci:n_in + ci + n_out], refs[n_in + ci + n_out:n_in + ci + n_out + co]
        scr, csems = refs[n_in + ci + n_out + co:n_in + ci + n_out + co + n_scr], refs[n_in + ci + n_out + co + n_scr:]
        lin = pl.program_id(0)
        for d in range(1, len(grid)):
            lin = lin * grid[d] + pl.program_id(d)
        pl.when(lin == 0)(lambda: merged.start(cins, couts, csems))
        body(*ins, *outs, *scr)
        mid_step = min((3 * total) // 4, total - 1)
        pl.when(lin == mid_step)(lambda: merged.mid(cins, couts, csems))
        pl.when(lin == total - 1)(lambda: merged.end(cins, couts, csems))

    any_spec = pl.BlockSpec(memory_space=pl.ANY)
    res = pl.pallas_call(
        wrapped, name=name, grid=grid, in_specs=list(in_specs) + [any_spec] * ci,
        out_specs=out_specs_l + [any_spec] * co, out_shape=out_shape_l + merged.out_shapes,
        scratch_shapes=scratch_shapes + merged.sems,
        compiler_params=_params(("arbitrary",) * len(grid)))(*args, *merged.inputs)
    outs, couts = res[:n_out], res[n_out:]
    return (outs[0] if single else list(outs)), [list(couts[a:b]) for a, b in spans]


def _comm_only(comm, name):
    def body():
        pass

    _, couts = _call(body, name=name, grid=(1,), in_specs=[], out_specs=[], out_shape=[], args=[], comm=comm)
    return couts[0]


def _tile(dim, pref):
    best = None
    for t in range(LANE, min(dim, pref) + 1, LANE):
        if dim % t == 0:
            best = t
    return best if best is not None else dim


MM_VMEM_BUDGET = 40 * 1024 * 1024


def _mm_tiles(m, n, k, a_bytes, b_bytes, o_bytes, r_bytes):
    for tk in (k, _tile(k, 2048), _tile(k, 1024), _tile(k, 512)):
        for tm, tn in ((1024, 1536), (1024, 1024), (1024, 512), (512, 512), (256, 512), (256, 256)):
            tm, tn = _tile(m, tm), _tile(n, tn)
            acc = 0 if tk == k else tm * tn * 4
            need = 2 * (tm * tk * a_bytes + tk * tn * b_bytes + tm * tn * (o_bytes + r_bytes)) + acc
            if need <= MM_VMEM_BUDGET:
                return tm, tn, tk
    raise ValueError("no matmul tiling fits VMEM")


def _mm(a, b, mode, name, out_dtype=F32, res=None, comm=None):
    if mode == "nn":
        (m, k), n = a.shape, b.shape[1]
    elif mode == "nt":
        (m, k), n = a.shape, b.shape[0]
    else:
        (k, m), n = a.shape, b.shape[1]
    tm, tn, tk = _mm_tiles(m, n, k, a.dtype.itemsize, b.dtype.itemsize, jnp.dtype(out_dtype).itemsize,
                           0 if res is None else res.dtype.itemsize)
    nk = k // tk
    dims = {"nn": ((1,), (0,)), "nt": ((1,), (1,)), "tn": ((0,), (0,))}[mode]

    def body(*refs):
        if res is None:
            a_ref, b_ref, o_ref = refs[:3]
            r_ref = None
        else:
            a_ref, b_ref, r_ref, o_ref = refs[:4]
        part = lax.dot_general(a_ref[...].astype(BF16), b_ref[...].astype(BF16), (dims, ((), ())),
                               preferred_element_type=F32)

        def finish(out):
            if r_ref is not None:
                out = out + r_ref[...].astype(F32)
            o_ref[...] = out.astype(out_dtype)

        if nk == 1:
            finish(part)
            return
        acc = refs[-1]
        kk = pl.program_id(2)

        @pl.when(kk == 0)
        def _():
            acc[...] = part

        @pl.when(kk > 0)
        def _():
            acc[...] += part

        @pl.when(kk == nk - 1)
        def _():
            finish(acc[...])

    a_spec = (pl.BlockSpec((tk, tm), lambda i, j, q: (q, i)) if mode == "tn"
              else pl.BlockSpec((tm, tk), lambda i, j, q: (i, q)))
    b_spec = (pl.BlockSpec((tn, tk), lambda i, j, q: (j, q)) if mode == "nt"
              else pl.BlockSpec((tk, tn), lambda i, j, q: (q, j)))
    o_spec = pl.BlockSpec((tm, tn), lambda i, j, q: (i, j))
    in_specs, args = [a_spec, b_spec], [a, b]
    if res is not None:
        in_specs.append(o_spec)
        args.append(res)
    return _call(body, name=name, grid=(m // tm, n // tn, nk), in_specs=in_specs, out_specs=o_spec,
                 out_shape=jax.ShapeDtypeStruct((m, n), out_dtype),
                 scratch_shapes=[] if nk == 1 else [pltpu.VMEM((tm, tn), F32)],
                 sem=("parallel", "parallel", "arbitrary"), args=args, comm=comm)


def _mm_parts(parts, b, mode, name, out_dtype=F32):
    count = len(parts)
    rows, cols = parts[0].shape
    n = b.shape[1]
    if mode == "nn":
        tm, tn, tk = _mm_tiles(rows, n, count * cols, parts[0].dtype.itemsize, b.dtype.itemsize,
                               jnp.dtype(out_dtype).itemsize, 0)
        assert tk == count * cols

        def body(*refs):
            a_refs, b_refs, o_ref = refs[:count], refs[count:2 * count], refs[2 * count]
            out = None
            for a_ref, b_ref in zip(a_refs, b_refs):
                part = jnp.dot(a_ref[...].astype(BF16), b_ref[...].astype(BF16), preferred_element_type=F32)
                out = part if out is None else out + part
            o_ref[...] = out.astype(out_dtype)

        return _call(body, name=name, grid=(rows // tm, n // tn),
                     in_specs=[pl.BlockSpec((tm, cols), lambda i, j: (i, 0))] * count
                     + [pl.BlockSpec((cols, tn), functools.partial(lambda q, i, j: (q, j), q)) for q in range(count)],
                     out_specs=pl.BlockSpec((tm, tn), lambda i, j: (i, j)),
                     out_shape=jax.ShapeDtypeStruct((rows, n), out_dtype),
                     sem=("parallel", "parallel"), args=(*parts, *([b] * count)))
    tm, tn = _tile(cols, 1536), _tile(n, 1024)
    per = cols // tm

    def body_t(*refs):
        a_refs, b_ref, o_ref = refs[:count], refs[count], refs[count + 1]
        for q, a_ref in enumerate(a_refs):
            @pl.when(pl.program_id(0) // per == q)
            def _(a_ref=a_ref):
                o_ref[...] = lax.dot_general(a_ref[...].astype(BF16), b_ref[...].astype(BF16),
                                             (((0,), (0,)), ((), ())), preferred_element_type=F32).astype(out_dtype)

    return _call(body_t, name=name, grid=(count * per, n // tn),
                 in_specs=[pl.BlockSpec((rows, tm), functools.partial(lambda q, i, j: (0, jnp.clip(i - q * per, 0, per - 1)), q))
                           for q in range(count)] + [pl.BlockSpec((rows, tn), lambda i, j: (0, j))],
                 out_specs=pl.BlockSpec((tm, tn), lambda i, j: (i, j)),
                 out_shape=jax.ShapeDtypeStruct((count * cols, n), out_dtype),
                 sem=("parallel", "parallel"), args=(*parts, b))


def _mm_bd(a, b, mode, name, out_dtype=F32, res=None, comm=None, tm=1024):
    if mode == "tn":
        k = a.shape[0]
        nb = min(a.shape[1], b.shape[1]) // LANE
        ma, n = a.shape[1] // nb, b.shape[1] // nb

        def body(a_ref, b_ref, o_ref):
            o_ref[0] = lax.dot_general(a_ref[...].astype(BF16), b_ref[...].astype(BF16), (((0,), (0,)), ((), ())),
                                       preferred_element_type=F32).astype(out_dtype)

        return _call(body, name=name, grid=(nb,),
                     in_specs=[pl.BlockSpec((k, ma), lambda j: (0, j)), pl.BlockSpec((k, n), lambda j: (0, j))],
                     out_specs=pl.BlockSpec((1, ma, n), lambda j: (j, 0, 0)),
                     out_shape=jax.ShapeDtypeStruct((nb, ma, n), out_dtype), sem=("parallel",), args=(a, b), comm=comm)
    m = a.shape[0]
    nb = b.shape[0]
    ka = a.shape[1] // nb
    n = b.shape[2] if mode == "nn" else b.shape[1]
    tm = _tile(m, tm)
    dims = ((1,), (0,)) if mode == "nn" else ((1,), (1,))

    def body(*refs):
        if res is None:
            a_ref, b_ref, o_ref = refs
            r_ref = None
        else:
            a_ref, b_ref, r_ref, o_ref = refs
        out = lax.dot_general(a_ref[...].astype(BF16), b_ref[0].astype(BF16), (dims, ((), ())),
                              preferred_element_type=F32)
        if r_ref is not None:
            out = out + r_ref[...].astype(F32)
        o_ref[...] = out.astype(out_dtype)

    o_spec = pl.BlockSpec((tm, n), lambda i, j: (i, j))
    in_specs = [pl.BlockSpec((tm, ka), lambda i, j: (i, j)), pl.BlockSpec((1,) + b.shape[1:], lambda i, j: (j, 0, 0))]
    args = [a, b]
    if res is not None:
        in_specs.append(o_spec)
        args.append(res)
    return _call(body, name=name, grid=(m // tm, nb), in_specs=in_specs, out_specs=o_spec,
                 out_shape=jax.ShapeDtypeStruct((m, nb * n), out_dtype), sem=("parallel", "parallel"),
                 args=args, comm=comm)


def _rms_fwd(x, g, out_dtype, name, tr=512):
    rows, d = x.shape
    tr = min(tr, rows)

    def body(x_ref, g_ref, o_ref):
        xv = x_ref[...]
        r = lax.rsqrt(jnp.mean(xv * xv, axis=-1, keepdims=True) + RMS_EPS)
        o_ref[...] = (xv * r * g_ref[...]).astype(out_dtype)

    return pl.pallas_call(
        body, name=name, grid=(rows // tr,),
        in_specs=[pl.BlockSpec((tr, d), lambda i: (i, 0)), pl.BlockSpec((1, d), lambda i: (0, 0))],
        out_specs=pl.BlockSpec((tr, d), lambda i: (i, 0)), out_shape=jax.ShapeDtypeStruct((rows, d), out_dtype),
        compiler_params=_params(("parallel",)))(x, g.reshape(1, d))


def _rms_bwd(x, g, dy, dres, name, tr=512, comm=None):
    rows, d = x.shape
    tr = min(tr, rows)

    def body(*refs):
        if dres is None:
            x_ref, g_ref, dy_ref, dx_ref, dg_ref = refs
            r_ref = None
        else:
            x_ref, g_ref, dy_ref, r_ref, dx_ref, dg_ref = refs

        @pl.when(pl.program_id(0) == 0)
        def _():
            dg_ref[...] = jnp.zeros_like(dg_ref)

        xv, dyv = x_ref[...], dy_ref[...].astype(F32)
        r = lax.rsqrt(jnp.mean(xv * xv, axis=-1, keepdims=True) + RMS_EPS)
        xh = xv * r
        dyg = dyv * g_ref[...]
        dx = r * (dyg - xh * jnp.mean(dyg * xh, axis=-1, keepdims=True))
        if r_ref is not None:
            dx = dx + r_ref[...]
        dx_ref[...] = dx
        dg_ref[...] += jnp.sum(dyv * xh, axis=0, keepdims=True)

    blk = pl.BlockSpec((tr, d), lambda i: (i, 0))
    vec = pl.BlockSpec((1, d), lambda i: (0, 0))
    in_specs, args = [blk, vec, blk], [x, g.reshape(1, d), dy]
    if dres is not None:
        in_specs.append(blk)
        args.append(dres)
    return _call(
        body, name=name, grid=(rows // tr,), in_specs=in_specs, out_specs=[blk, vec],
        out_shape=[jax.ShapeDtypeStruct((rows, d), F32), jax.ShapeDtypeStruct((1, d), F32)],
        sem=("arbitrary",), args=args, comm=comm)


def _loss_head(x, g, target, name, tr=512):
    rows, d = x.shape

    def body(x_ref, g_ref, t_ref, loss_ref, dx_ref, dg_ref):
        @pl.when(pl.program_id(0) == 0)
        def _():
            dg_ref[...] = jnp.zeros_like(dg_ref)
            loss_ref[...] = jnp.zeros_like(loss_ref)

        xv = x_ref[...]
        r = lax.rsqrt(jnp.mean(xv * xv, axis=-1, keepdims=True) + RMS_EPS)
        xh = xv * r
        err = xh * g_ref[...] - t_ref[...]
        loss_ref[...] += 0.5 * jnp.sum(jnp.mean(err * err, axis=-1, keepdims=True), keepdims=True)
        dyv = err * (1.0 / d)
        dyg = dyv * g_ref[...]
        dx_ref[...] = r * (dyg - xh * jnp.mean(dyg * xh, axis=-1, keepdims=True))
        dg_ref[...] += jnp.sum(dyv * xh, axis=0, keepdims=True)

    blk = pl.BlockSpec((tr, d), lambda i: (i, 0))
    vec = pl.BlockSpec((1, d), lambda i: (0, 0))
    return pl.pallas_call(
        body, name=name, grid=(rows // tr,), in_specs=[blk, vec, blk],
        out_specs=[pl.BlockSpec((1, 1), lambda i: (0, 0)), blk, vec],
        out_shape=[jax.ShapeDtypeStruct((1, 1), F32), jax.ShapeDtypeStruct((rows, d), F32),
                   jax.ShapeDtypeStruct((1, d), F32)],
        compiler_params=_params(("arbitrary",)))(x, g.reshape(1, d), target)


def _shift_down(x, s):
    rows = lax.broadcasted_iota(jnp.int32, x.shape, 0)
    return jnp.where(rows >= s, pltpu.roll(x, s, 0), 0.0)


def _shift_up(x, s):
    n = x.shape[0]
    rows = lax.broadcasted_iota(jnp.int32, x.shape, 0)
    return jnp.where(rows < n - s, pltpu.roll(x, n - s, 0), 0.0)


def _sigmoid(x):
    return 1.0 / (1.0 + jnp.exp(-x))


def _silu_and_grad(x):
    s = _sigmoid(x)
    return x * s, s * (1.0 + x * (1.0 - s))


_GELU_C0, _GELU_C1 = math.sqrt(2.0 / math.pi), 0.044715


def _gelu_and_grad(x):
    th = jnp.tanh(_GELU_C0 * (x + _GELU_C1 * x * x * x))
    y = 0.5 * x * (1.0 + th)
    dy = 0.5 * (1.0 + th) + 0.5 * x * (1.0 - th * th) * _GELU_C0 * (1.0 + 3.0 * _GELU_C1 * x * x)
    return y, dy


def _ffn_act_fwd(h, w, name, tc=256, comm=None):
    t = h.shape[0]
    nb = D_FF // tc

    def body(hg_ref, hv_ref, wg_ref, wv_ref, a_ref):
        def conv(x, wr):
            return wr[2:3, :] * x + wr[1:2, :] * _shift_down(x, 1) + wr[0:1, :] * _shift_down(x, 2)

        cg = conv(hg_ref[...], wg_ref[...])
        cv = conv(hv_ref[...], wv_ref[...])
        a_ref[...] = (cg * _sigmoid(cg) * cv).astype(BF16)

    return _call(
        body, name=name, grid=(nb,),
        in_specs=[pl.BlockSpec((t, tc), lambda j: (0, j)), pl.BlockSpec((t, tc), lambda j: (0, j + nb)),
                  pl.BlockSpec((CONV_FFN, tc), lambda j: (0, j)), pl.BlockSpec((CONV_FFN, tc), lambda j: (0, j + nb))],
        out_specs=pl.BlockSpec((t, tc), lambda j: (0, j)), out_shape=jax.ShapeDtypeStruct((t, D_FF), BF16),
        sem=("parallel",), args=(h, h, w, w), comm=comm)


def _ffn_act_bwd(h, w, da, name, tc=256, comm=None):
    t = h.shape[0]
    nb = D_FF // tc

    def body(hg_ref, hv_ref, wg_ref, wv_ref, da_ref, dhg_ref, dhv_ref, dwg_ref, dwv_ref):
        hg, hv, wg, wv = hg_ref[...], hv_ref[...], wg_ref[...], wv_ref[...]
        hg1, hg2, hv1, hv2 = _shift_down(hg, 1), _shift_down(hg, 2), _shift_down(hv, 1), _shift_down(hv, 2)
        cg = wg[2:3, :] * hg + wg[1:2, :] * hg1 + wg[0:1, :] * hg2
        cv = wv[2:3, :] * hv + wv[1:2, :] * hv1 + wv[0:1, :] * hv2
        sg, dsg = _silu_and_grad(cg)
        dav = da_ref[...].astype(F32)
        dcv = dav * sg
        dcg = dav * cv * dsg

        def conv_t(dc, wr):
            return wr[2:3, :] * dc + wr[1:2, :] * _shift_up(dc, 1) + wr[0:1, :] * _shift_up(dc, 2)

        dhg_ref[...] = conv_t(dcg, wg).astype(BF16)
        dhv_ref[...] = conv_t(dcv, wv).astype(BF16)
        dwg_ref[0:1, :] = jnp.sum(dcg * hg2, axis=0, keepdims=True)
        dwg_ref[1:2, :] = jnp.sum(dcg * hg1, axis=0, keepdims=True)
        dwg_ref[2:3, :] = jnp.sum(dcg * hg, axis=0, keepdims=True)
        dwv_ref[0:1, :] = jnp.sum(dcv * hv2, axis=0, keepdims=True)
        dwv_ref[1:2, :] = jnp.sum(dcv * hv1, axis=0, keepdims=True)
        dwv_ref[2:3, :] = jnp.sum(dcv * hv, axis=0, keepdims=True)

    big = lambda off: pl.BlockSpec((t, tc), lambda j: (0, j + off))
    small = lambda off: pl.BlockSpec((CONV_FFN, tc), lambda j: (0, j + off))
    res = _call(
        body, name=name, grid=(nb,),
        in_specs=[big(0), big(nb), small(0), small(nb), big(0)],
        out_specs=[big(0), big(0), small(0), small(0)],
        out_shape=[jax.ShapeDtypeStruct((t, D_FF), BF16), jax.ShapeDtypeStruct((t, D_FF), BF16),
                   jax.ShapeDtypeStruct((CONV_FFN, D_FF), F32), jax.ShapeDtypeStruct((CONV_FFN, D_FF), F32)],
        sem=("parallel",), args=(h, h, w, w, da), comm=comm)
    (dhg, dhv, dwg, dwv), couts = res if comm is not None else (res, None)
    out = ((dhg, dhv), jnp.concatenate([dwg, dwv], axis=1))
    return out if comm is None else (out, couts)


def _attn_probs(q, k):
    s = lax.dot_general(q.astype(BF16), k.astype(BF16), (((1,), (1,)), ((), ())),
                        preferred_element_type=F32) * (HEAD_X ** -0.5)
    s = s - jnp.max(s, axis=-1, keepdims=True)
    p = jnp.exp(s)
    return p / jnp.sum(p, axis=-1, keepdims=True)


def _attn_fwd(q, kv, name, tq=2048, comm=None):
    t = q.shape[0]

    def body(q_ref, k_ref, v_ref, o_ref):
        p = _attn_probs(q_ref[...], k_ref[...])
        o_ref[...] = jnp.dot(p.astype(BF16), v_ref[...].astype(BF16), preferred_element_type=F32).astype(BF16)

    return _call(
        body, name=name, grid=(N_HEADS_X, t // tq),
        in_specs=[pl.BlockSpec((tq, HEAD_X), lambda h, i: (i, h)),
                  pl.BlockSpec((MEM_LEN, HEAD_X), lambda h, i: (0, h)),
                  pl.BlockSpec((MEM_LEN, HEAD_X), lambda h, i: (0, h + N_HEADS_X))],
        out_specs=pl.BlockSpec((tq, HEAD_X), lambda h, i: (i, h)),
        out_shape=jax.ShapeDtypeStruct((t, N_HEADS_X * HEAD_X), BF16),
        sem=("parallel", "parallel"), args=(q, kv, kv), comm=comm)


def _attn_bwd(q, kv, do, name, tq=2048):
    t = q.shape[0]

    def body(q_ref, k_ref, v_ref, do_ref, dq_ref, dk_ref, dv_ref):
        @pl.when(pl.program_id(1) == 0)
        def _():
            dk_ref[...] = jnp.zeros_like(dk_ref)
            dv_ref[...] = jnp.zeros_like(dv_ref)

        qb, kb, vb, dob = (r[...].astype(BF16) for r in (q_ref, k_ref, v_ref, do_ref))
        p = _attn_probs(qb, kb)
        dp = lax.dot_general(dob, vb, (((1,), (1,)), ((), ())), preferred_element_type=F32)
        ds = p * (dp - jnp.sum(dp * p, axis=-1, keepdims=True)) * (HEAD_X ** -0.5)
        dsb = ds.astype(BF16)
        dq_ref[...] = jnp.dot(dsb, kb, preferred_element_type=F32).astype(BF16)
        dk_ref[...] += lax.dot_general(dsb, qb, (((0,), (0,)), ((), ())), preferred_element_type=F32)
        dv_ref[...] += lax.dot_general(p.astype(BF16), dob, (((0,), (0,)), ((), ())), preferred_element_type=F32)

    qs = pl.BlockSpec((tq, HEAD_X), lambda h, i: (i, h))
    ms = pl.BlockSpec((MEM_LEN, HEAD_X), lambda h, i: (0, h))
    return pl.pallas_call(
        body, name=name, grid=(N_HEADS_X, t // tq),
        in_specs=[qs, ms, pl.BlockSpec((MEM_LEN, HEAD_X), lambda h, i: (0, h + N_HEADS_X)), qs],
        out_specs=[qs, ms, ms],
        out_shape=[jax.ShapeDtypeStruct((t, D_MODEL), BF16), jax.ShapeDtypeStruct((MEM_LEN, D_MODEL), F32),
                   jax.ShapeDtypeStruct((MEM_LEN, D_MODEL), F32)],
        compiler_params=_params(("parallel", "arbitrary")))(q, kv, kv, do)


def _pool_counts(t, win):
    pos = lax.broadcasted_iota(jnp.int32, (t, 1), 0).astype(F32) + 1.0
    return 1.0 / jnp.minimum(pos, float(win))


def _pool_delta(xv, win):
    s, step = xv, 1
    while step < win:
        s = s + _shift_down(s, step)
        step *= 2
    return s * _pool_counts(xv.shape[0], win) - xv


def _pool_delta_t(dv, win):
    s, step = dv * _pool_counts(dv.shape[0], win), 1
    while step < win:
        s = s + _shift_up(s, step)
        step *= 2
    return s - dv


def _pool_fwd(xn, w, scale, res, name):
    t = xn.shape[0]

    def make_branch(win, xn_ref, w_ref, s_ref, r_ref, o_ref):
        def branch():
            dl = _pool_delta(xn_ref[...], win)
            y = jnp.dot(dl.astype(BF16), w_ref[0], preferred_element_type=F32)
            o_ref[...] = r_ref[...] + y * s_ref[...]
        return branch

    def body(xn_ref, w_ref, s_ref, r_ref, o_ref):
        for gi, win in enumerate(POOL_WINDOWS):
            pl.when(pl.program_id(0) == gi)(make_branch(win, xn_ref, w_ref, s_ref, r_ref, o_ref))

    blk = pl.BlockSpec((t, POOL_GROUP), lambda g: (0, g))
    return pl.pallas_call(
        body, name=name, grid=(len(POOL_WINDOWS),),
        in_specs=[blk, pl.BlockSpec((1, POOL_GROUP, POOL_GROUP), lambda g: (g, 0, 0)),
                  pl.BlockSpec((1, POOL_GROUP), lambda g: (0, g)), blk],
        out_specs=blk, out_shape=jax.ShapeDtypeStruct((t, D_MODEL), F32),
        compiler_params=_params(("parallel",)))(xn, w, scale, res)


def _pool_bwd(xn, w, scale, dmix, name):
    t = xn.shape[0]

    def make_branch(win, xn_ref, w_ref, s_ref, d_ref, dxn_ref, dw_ref, ds_ref):
        def branch():
            dl = _pool_delta(xn_ref[...], win).astype(BF16)
            wv = w_ref[0]
            dm = d_ref[...]
            y = jnp.dot(dl, wv, preferred_element_type=F32)
            ds_ref[...] = jnp.sum(dm * y, axis=0, keepdims=True)
            dy = (dm * s_ref[...]).astype(BF16)
            dw_ref[0] = lax.dot_general(dl, dy, (((0,), (0,)), ((), ())), preferred_element_type=F32)
            ddl = lax.dot_general(dy, wv, (((1,), (1,)), ((), ())), preferred_element_type=F32)
            dxn_ref[...] = _pool_delta_t(ddl, win)
        return branch

    def body(*refs):
        for gi, win in enumerate(POOL_WINDOWS):
            pl.when(pl.program_id(0) == gi)(make_branch(win, *refs))

    blk = pl.BlockSpec((t, POOL_GROUP), lambda g: (0, g))
    wspec = pl.BlockSpec((1, POOL_GROUP, POOL_GROUP), lambda g: (g, 0, 0))
    vec = pl.BlockSpec((1, POOL_GROUP), lambda g: (0, g))
    return pl.pallas_call(
        body, name=name, grid=(len(POOL_WINDOWS),), in_specs=[blk, wspec, vec, blk], out_specs=[blk, wspec, vec],
        out_shape=[jax.ShapeDtypeStruct((t, D_MODEL), F32),
                   jax.ShapeDtypeStruct((len(POOL_WINDOWS), POOL_GROUP, POOL_GROUP), F32),
                   jax.ShapeDtypeStruct((1, D_MODEL), F32)],
        compiler_params=_params(("parallel",)))(xn, w, scale, dmix)


def _qkv_conv(h, wr):
    return (wr[3:4, :] * h + wr[2:3, :] * _shift_down(h, 1) + wr[1:2, :] * _shift_down(h, 2)
            + wr[0:1, :] * _shift_down(h, 3))


def _qkv_block_kind(j):
    return j < 2 * N_HEADS_A, jnp.where(j < N_HEADS_A, HEAD_A ** -0.5, 1.0)


def _qkv_pre_fwd(h, w, name):
    t, cols = h.shape

    def body(h_ref, w_ref, o_ref):
        normalised, scale = _qkv_block_kind(pl.program_id(0))
        c = _qkv_conv(h_ref[...], w_ref[...])
        s = c * _sigmoid(c)
        r = lax.rsqrt(jnp.sum(s * s, axis=-1, keepdims=True) + 1e-6)
        o_ref[...] = jnp.where(normalised, s * (r * scale), s)

    blk = pl.BlockSpec((t, HEAD_A), lambda j: (0, j))
    return pl.pallas_call(
        body, name=name, grid=(cols // HEAD_A,), in_specs=[blk, pl.BlockSpec((CONV_A, HEAD_A), lambda j: (0, j))],
        out_specs=blk, out_shape=jax.ShapeDtypeStruct((t, cols), F32), compiler_params=_params(("parallel",)))(h, w)


def _qkv_pre_bwd(h, w, dy, name):
    t, cols = h.shape

    def body(h_ref, w_ref, dy_ref, dh_ref, dw_ref):
        normalised, scale = _qkv_block_kind(pl.program_id(0))
        hv, wr, dyv = h_ref[...], w_ref[...], dy_ref[...]
        h1, h2, h3 = _shift_down(hv, 1), _shift_down(hv, 2), _shift_down(hv, 3)
        c = wr[3:4, :] * hv + wr[2:3, :] * h1 + wr[1:2, :] * h2 + wr[0:1, :] * h3
        s, dsilu = _silu_and_grad(c)
        r = lax.rsqrt(jnp.sum(s * s, axis=-1, keepdims=True) + 1e-6)
        y = s * r
        dys = dyv * scale
        ds = jnp.where(normalised, r * (dys - y * jnp.sum(dys * y, axis=-1, keepdims=True)), dyv)
        dc = ds * dsilu
        dh = (wr[3:4, :] * dc + wr[2:3, :] * _shift_up(dc, 1) + wr[1:2, :] * _shift_up(dc, 2)
              + wr[0:1, :] * _shift_up(dc, 3))
        dh_ref[...] = dh.astype(BF16)
        dw_ref[0:1, :] = jnp.sum(dc * h3, axis=0, keepdims=True)
        dw_ref[1:2, :] = jnp.sum(dc * h2, axis=0, keepdims=True)
        dw_ref[2:3, :] = jnp.sum(dc * h1, axis=0, keepdims=True)
        dw_ref[3:4, :] = jnp.sum(dc * hv, axis=0, keepdims=True)

    blk = pl.BlockSpec((t, HEAD_A), lambda j: (0, j))
    taps = pl.BlockSpec((CONV_A, HEAD_A), lambda j: (0, j))
    return pl.pallas_call(
        body, name=name, grid=(cols // HEAD_A,), in_specs=[blk, taps, blk], out_specs=[blk, taps],
        out_shape=[jax.ShapeDtypeStruct((t, cols), BF16), jax.ShapeDtypeStruct((CONV_A, cols), F32)],
        compiler_params=_params(("parallel",)))(h, w, dy)


def _softplus(x):
    return jnp.maximum(x, 0.0) + jnp.log1p(jnp.exp(-jnp.abs(x)))


def _gates_fwd(ba, arow, brow, name):
    t = ba.shape[0]

    def body(x_ref, a_ref, b_ref, o_ref):
        xv = x_ref[...]
        lane = lax.broadcasted_iota(jnp.int32, xv.shape, 1)
        beta = _sigmoid(xv)
        g = -jnp.exp(a_ref[...]) * _softplus(xv + b_ref[...])
        o_ref[...] = jnp.where(lane < N_HEADS_A, beta, jnp.where(lane < 2 * N_HEADS_A, g, 0.0))

    return pl.pallas_call(body, name=name, out_shape=jax.ShapeDtypeStruct((t, LANE), F32),
                          compiler_params=_params())(ba, arow, brow)


def _gates_bwd(ba, arow, brow, dgb, name):
    t = ba.shape[0]

    def body(x_ref, a_ref, b_ref, d_ref, dx_ref, da_ref, db_ref):
        xv = x_ref[...]
        dv = d_ref[0] + d_ref[1] + d_ref[2] + d_ref[3]
        lane = lax.broadcasted_iota(jnp.int32, xv.shape, 1)
        beta = _sigmoid(xv)
        ea = jnp.exp(a_ref[...])
        z = xv + b_ref[...]
        dgv = jnp.where((lane >= N_HEADS_A) & (lane < 2 * N_HEADS_A), dv, 0.0) * (-ea)
        dz = dgv * _sigmoid(z)
        dx = jnp.where(lane < N_HEADS_A, dv * beta * (1.0 - beta), dz)
        dx_ref[...] = dx.astype(BF16)
        db_ref[...] = jnp.sum(dz, axis=0, keepdims=True)
        da_ref[...] = jnp.sum(dgv * _softplus(z), axis=0, keepdims=True)

    return pl.pallas_call(
        body, name=name,
        out_shape=[jax.ShapeDtypeStruct((t, LANE), BF16), jax.ShapeDtypeStruct((1, LANE), F32),
                   jax.ShapeDtypeStruct((1, LANE), F32)],
        compiler_params=_params())(ba, arow, brow, dgb)


def _head_gates(gates, head):
    lane = lax.broadcasted_iota(jnp.int32, gates.shape, 1)
    beta = jnp.sum(jnp.where(lane == head, gates, 0.0), axis=1, keepdims=True)
    g = jnp.sum(jnp.where(lane == head + N_HEADS_A, gates, 0.0), axis=1, keepdims=True)
    return beta, g


_B_NN, _B_NT, _B_TN = ((2,), (1,)), ((2,), (2,)), ((1,), (1,))


def _bdot(a, b, dims=_B_NN, prec=None):
    if prec is None:
        a, b = a.astype(BF16), b.astype(BF16)
    return lax.dot_general(a, b, (dims, ((0,), (0,))), precision=prec, preferred_element_type=F32)


def _heads_of(ref):
    return jnp.stack([ref[:, h * HEAD_A:(h + 1) * HEAD_A] for h in range(N_HEADS_A)])


def _all_head_gates(gates):
    pairs = [_head_gates(gates, h) for h in range(N_HEADS_A)]
    return jnp.stack([b for b, _ in pairs]), jnp.stack([g for _, g in pairs])


def _gdr_terms(k, beta, g):
    h, c = k.shape[0], GDR_CHUNK
    row = lax.broadcasted_iota(jnp.int32, (c, c), 0)
    col = lax.broadcasted_iota(jnp.int32, (c, c), 1)
    causal, strict = row >= col, row > col
    lower = jnp.broadcast_to(causal.astype(F32), (h, c, c))
    gcum = _bdot(lower, jnp.broadcast_to(g, (h, c, c)), prec=HIGHEST)
    diff = gcum - jnp.swapaxes(gcum, 1, 2)
    decay = jnp.where(causal, jnp.exp(jnp.where(causal, diff, 0.0)), 0.0)
    kb = k * beta
    return row, col, causal, strict, gcum, decay, kb, _bdot(kb, k, _B_NT)


def _unit_lower_inverses(a):
    c = a.shape[1]
    eye = (lax.broadcasted_iota(jnp.int32, (c, c), 0) == lax.broadcasted_iota(jnp.int32, (c, c), 1)).astype(F32)
    p = -a
    inv = eye + p
    step = 1
    while 2 * step < c:
        p = _bdot(p, p, prec=HIGH)
        inv = inv + _bdot(inv, p, prec=HIGH)
        step *= 2
    return inv


def _gdr_fwd(qkv, gates, name, comm=None):
    t = qkv.shape[0]
    c, nh = GDR_CHUNK, N_HEADS_A
    n = t // c

    def body(q_ref, k_ref, v_ref, gb_ref, o_ref, tm_ref, s_ref, state):
        @pl.when(pl.program_id(0) == 0)
        def _():
            state[...] = jnp.zeros_like(state)

        qv, kv, vv = _heads_of(q_ref), _heads_of(k_ref), _heads_of(v_ref)
        beta, g = _all_head_gates(gb_ref[...])
        row, col, causal, strict, gcum, decay, kb, kk = _gdr_terms(kv, beta, g)
        tm = _unit_lower_inverses(jnp.where(strict, kk * decay, 0.0))
        e = jnp.exp(gcum)
        u = _bdot(tm, vv * beta, prec=HIGH)
        w = _bdot(tm, kb * e, prec=HIGH)
        p = jnp.where(causal, _bdot(qv, kv, _B_NT) * decay, 0.0)
        s = state[...]
        s_ref[:, 0] = s
        tm_ref[:, 0] = tm
        vn = u - _bdot(w, s)
        o = _bdot(qv * e, s) + _bdot(p, vn)
        for h in range(nh):
            o_ref[:, h * HEAD_A:(h + 1) * HEAD_A] = o[h]
        glast = gcum[:, c - 1:c, :]
        state[...] = s * jnp.exp(glast) + _bdot(kv * jnp.exp(glast - gcum), vn, _B_TN)

    part = lambda p: pl.BlockSpec((c, WIDTH_A), lambda i: (i, p))
    mat = pl.BlockSpec((nh, 1, c, c), lambda i: (0, i, 0, 0))
    return _call(
        body, name=name, grid=(n,), in_specs=[part(0), part(1), part(2), pl.BlockSpec((c, LANE), lambda i: (i, 0))],
        out_specs=[part(0), mat, mat],
        out_shape=[jax.ShapeDtypeStruct((t, WIDTH_A), F32), jax.ShapeDtypeStruct((nh, n, c, c), F32),
                   jax.ShapeDtypeStruct((nh, n, HEAD_A, HEAD_A), F32)],
        scratch_shapes=[pltpu.VMEM((nh, HEAD_A, HEAD_A), F32)], sem=("arbitrary",),
        args=(qkv, qkv, qkv, gates), comm=comm)


def _gdr_bwd(qkv, gates, tm_all, s_all, do, name, comm=None):
    t = qkv.shape[0]
    c, nh = GDR_CHUNK, N_HEADS_A
    n = t // c

    def body(q_ref, k_ref, v_ref, gb_ref, tm_ref, s_ref, do_ref, dqkv_ref, dgb_ref, dstate):
        @pl.when(pl.program_id(0) == 0)
        def _():
            dstate[...] = jnp.zeros_like(dstate)

        qv, kv, vv, dov = _heads_of(q_ref), _heads_of(k_ref), _heads_of(v_ref), _heads_of(do_ref)
        beta, g = _all_head_gates(gb_ref[...])
        tm, s, dsp = tm_ref[:, 0], s_ref[:, 0], dstate[...]
        row, col, causal, strict, gcum, decay, kb, kk = _gdr_terms(kv, beta, g)
        rowsum = lambda x: jnp.sum(x, axis=2, keepdims=True)
        e = jnp.exp(gcum)
        vb, kbe = vv * beta, kb * e
        u = _bdot(tm, vb, prec=HIGH)
        w = _bdot(tm, kbe, prec=HIGH)
        qk = _bdot(qv, kv, _B_NT)
        p = jnp.where(causal, qk * decay, 0.0)
        vn = u - _bdot(w, s)
        glast = gcum[:, c - 1:c, :]
        el = jnp.exp(glast)
        f = jnp.exp(glast - gcum)
        kd = kv * f
        qe = qv * e

        dvn = _bdot(p, dov, _B_TN) + _bdot(kd, dsp)
        dglast = el[:, :, 0:1] * jnp.sum(s * dsp, axis=(1, 2), keepdims=True)
        dkd = _bdot(vn, dsp, _B_NT)
        dk = dkd * f
        df = rowsum(dkd * kv) * f[:, :, 0:1]
        dglast = dglast + jnp.sum(df, axis=1, keepdims=True)
        dgc = -df
        dp = jnp.where(causal, _bdot(dov, vn, _B_NT), 0.0)
        dqe = _bdot(dov, s, _B_NT)
        dq = dqe * e
        de = rowsum(dqe * qv)
        dstate[...] = dsp * el + _bdot(qe, dov, _B_TN) - _bdot(w, dvn, _B_TN)
        dw = -_bdot(dvn, s, _B_NT)
        dvb = _bdot(tm, dvn, _B_TN, prec=HIGH)
        dkbe = _bdot(tm, dw, _B_TN, prec=HIGH)
        da = -jnp.where(strict, _bdot(dvb, u, _B_NT) + _bdot(dkbe, w, _B_NT), 0.0)
        dkk = da * decay
        dqk = dp * decay
        dd = da * kk + dp * qk
        dq = dq + _bdot(dqk, kv)
        dk = dk + _bdot(dqk, qv, _B_TN)
        dkb = _bdot(dkk, kv) + dkbe * e
        dk = dk + _bdot(dkk, kb, _B_TN)
        de = de + rowsum(dkbe * kb)
        dk = dk + dkb * beta
        dbeta = rowsum(dkb * kv) + rowsum(dvb * vv)
        m = dd * decay
        dgc = dgc + rowsum(m) - rowsum(jnp.swapaxes(m, 1, 2))
        dgc = dgc + de * e[:, :, 0:1]
        dgc = dgc + jnp.where(row[:, 0:1] == c - 1, dglast, 0.0)
        upper = jnp.broadcast_to((row <= col).astype(F32), (nh, c, c))
        dg = _bdot(upper, jnp.broadcast_to(dgc, (nh, c, c)), prec=HIGHEST)
        dv = dvb * beta
        for p, grad in enumerate((dq, dk, dv)):
            for h in range(nh):
                dqkv_ref[:, p * WIDTH_A + h * HEAD_A:p * WIDTH_A + (h + 1) * HEAD_A] = grad[h]
        head = lax.broadcasted_iota(jnp.int32, (nh, c, LANE), 0)
        lane = lax.broadcasted_iota(jnp.int32, (nh, c, LANE), 2)
        dgb_ref[...] = jnp.where(lane == head, dbeta, jnp.where(lane == head + nh, dg, 0.0))

    part = lambda p: pl.BlockSpec((c, WIDTH_A), lambda i: (n - 1 - i, p))
    mat = pl.BlockSpec((nh, 1, c, c), lambda i: (0, n - 1 - i, 0, 0))
    return _call(
        body, name=name, grid=(n,),
        in_specs=[part(0), part(1), part(2), pl.BlockSpec((c, LANE), lambda i: (n - 1 - i, 0)), mat, mat, part(0)],
        out_specs=[pl.BlockSpec((c, 3 * WIDTH_A), lambda i: (n - 1 - i, 0)),
                   pl.BlockSpec((nh, c, LANE), lambda i: (0, n - 1 - i, 0))],
        out_shape=[jax.ShapeDtypeStruct((t, 3 * WIDTH_A), F32), jax.ShapeDtypeStruct((nh, t, LANE), F32)],
        scratch_shapes=[pltpu.VMEM((nh, HEAD_A, HEAD_A), F32)], sem=("arbitrary",),
        args=(qkv, qkv, qkv, gates, tm_all, s_all, do), comm=comm)


def _onorm_fwd(o, gate, g, name):
    t = o.shape[0]

    def body(o_ref, gate_ref, g_ref, y_ref):
        ov, gv = o_ref[...], gate_ref[...]
        r = lax.rsqrt(jnp.mean(ov * ov, axis=-1, keepdims=True) + RMS_EPS)
        y_ref[...] = (ov * r * g_ref[...] * gv * _sigmoid(gv)).astype(BF16)

    blk = pl.BlockSpec((t, HEAD_A), lambda j: (0, j))
    return pl.pallas_call(
        body, name=name, grid=(N_HEADS_A,), in_specs=[blk, blk, pl.BlockSpec((1, HEAD_A), lambda j: (0, 0))],
        out_specs=blk, out_shape=jax.ShapeDtypeStruct((t, WIDTH_A), BF16),
        compiler_params=_params(("parallel",)))(o, gate, g)


def _onorm_bwd(o, gate, g, dy, name):
    t = o.shape[0]

    def body(o_ref, gate_ref, g_ref, dy_ref, do_ref, dgate_ref, dg_ref):
        @pl.when(pl.program_id(0) == 0)
        def _():
            dg_ref[...] = jnp.zeros_like(dg_ref)

        ov, gv, dyv = o_ref[...], gate_ref[...], dy_ref[...].astype(F32)
        r = lax.rsqrt(jnp.mean(ov * ov, axis=-1, keepdims=True) + RMS_EPS)
        oh = ov * r
        sg, dsg = _silu_and_grad(gv)
        dgate_ref[...] = (dyv * oh * g_ref[...] * dsg).astype(BF16)
        dn = dyv * sg
        dg_ref[...] += jnp.sum(dn * oh, axis=0, keepdims=True)
        dng = dn * g_ref[...]
        do_ref[...] = r * (dng - oh * jnp.mean(dng * oh, axis=-1, keepdims=True))

    blk = pl.BlockSpec((t, HEAD_A), lambda j: (0, j))
    vec = pl.BlockSpec((1, HEAD_A), lambda j: (0, 0))
    return pl.pallas_call(
        body, name=name, grid=(N_HEADS_A,), in_specs=[blk, blk, vec, blk], out_specs=[blk, blk, vec],
        out_shape=[jax.ShapeDtypeStruct((t, WIDTH_A), F32), jax.ShapeDtypeStruct((t, WIDTH_A), BF16),
                   jax.ShapeDtypeStruct((1, HEAD_A), F32)],
        compiler_params=_params(("arbitrary",)))(o, gate, g, dy)


def _cmul(ar, ai, br, bi):
    return ar * br - ai * bi, ar * bi + ai * br


def _scan_tables(ar, ai, reverse):
    p1 = (ar, ai)
    p2 = _cmul(*p1, *p1)
    p4 = _cmul(*p2, *p2)
    p8 = _cmul(*p4, *p4)
    p3 = _cmul(*p2, *p1)
    p5 = _cmul(*p4, *p1)
    p6 = _cmul(*p4, *p2)
    p7 = _cmul(*p4, *p3)
    pows = [p1, p2, p3, p4, p5, p6, p7, p8]
    rows = lax.broadcasted_iota(jnp.int32, (8, ar.shape[1]), 0)
    tr = jnp.zeros((8, ar.shape[1]), F32)
    ti = jnp.zeros((8, ar.shape[1]), F32)
    for r in range(8):
        pw = pows[7 - r] if reverse else pows[r]
        tr = jnp.where(rows == r, pw[0], tr)
        ti = jnp.where(rows == r, pw[1], ti)
    return p1, p2, p4, p8, tr, ti


def _tile_scan(xr, xi, p1, p2, p4, reverse):
    rows = lax.broadcasted_iota(jnp.int32, xr.shape, 0)
    for s, (pr, pi) in ((1, p1), (2, p2), (4, p4)):
        if reverse:
            keep = rows < 8 - s
            sr, si = pltpu.roll(xr, 8 - s, 0), pltpu.roll(xi, 8 - s, 0)
        else:
            keep = rows >= s
            sr, si = pltpu.roll(xr, s, 0), pltpu.roll(xi, s, 0)
        sr, si = jnp.where(keep, sr, 0.0), jnp.where(keep, si, 0.0)
        mr, mi = _cmul(pr, pi, sr, si)
        xr, xi = xr + mr, xi + mi
    return xr, xi


def _s5_scan_fwd(bu, a, name, tb=512, comm=None):
    t = bu.shape[0]
    cb = SCAN_CB
    nt = t // tb

    def body(b_ref, a_ref, x_ref, carry):
        @pl.when(pl.program_id(1) == 0)
        def _():
            carry[...] = jnp.zeros_like(carry)

        ar, ai = a_ref[:, 0:cb], a_ref[:, cb:2 * cb]
        p1, p2, p4, p8, tr, ti = _scan_tables(ar, ai, False)

        def step(j, c):
            cr, ci = c
            i = pl.multiple_of(j * 8, 8)
            xr, xi = _tile_scan(b_ref[pl.ds(i, 8), 0:cb], b_ref[pl.ds(i, 8), cb:2 * cb], p1, p2, p4, False)
            mr, mi = _cmul(tr, ti, cr, ci)
            xr, xi = xr + mr, xi + mi
            x_ref[pl.ds(i, 8), 0:cb] = xr
            x_ref[pl.ds(i, 8), cb:2 * cb] = xi
            return xr[7:8, :], xi[7:8, :]

        cr, ci = lax.fori_loop(0, tb // 8, step, (carry[0:1, :], carry[1:2, :]), unroll=2)
        carry[0:1, :] = cr
        carry[1:2, :] = ci

    blk = pl.BlockSpec((tb, 2 * cb), lambda j, i: (i, j))
    return _call(
        body, name=name, grid=(SSM_CH // cb, nt),
        in_specs=[blk, pl.BlockSpec((1, 2 * cb), lambda j, i: (0, j))], out_specs=blk,
        out_shape=jax.ShapeDtypeStruct((t, 2 * SSM_CH), F32), scratch_shapes=[pltpu.VMEM((8, cb), F32)],
        sem=("parallel", "arbitrary"), args=(bu, a), comm=comm)


def _s5_scan_bwd(dx, x, a, name, tb=512, comm=None):
    t = dx.shape[0]
    cb = SCAN_CB
    nt = t // tb
    nj = tb // 8

    def body(d_ref, x_ref, xp_ref, a_ref, l_ref, da_ref, carry, acc):
        tblk = pl.program_id(1)

        @pl.when(tblk == 0)
        def _():
            carry[...] = jnp.zeros_like(carry)
            acc[...] = jnp.zeros_like(acc)

        ar, ai = a_ref[:, 0:cb], a_ref[:, cb:2 * cb]
        p1, p2, p4, p8, tr, ti = _scan_tables(ar, -ai, True)
        rows = lax.broadcasted_iota(jnp.int32, (8, cb), 0)

        def step(jj, c):
            cr, ci, sr_acc, si_acc = c
            j = nj - 1 - jj
            i = pl.multiple_of(j * 8, 8)
            lr, li = _tile_scan(d_ref[pl.ds(i, 8), 0:cb], d_ref[pl.ds(i, 8), cb:2 * cb], p1, p2, p4, True)
            mr, mi = _cmul(tr, ti, cr, ci)
            lr, li = lr + mr, li + mi
            l_ref[pl.ds(i, 8), 0:cb] = lr
            l_ref[pl.ds(i, 8), cb:2 * cb] = li
            ip = pl.multiple_of(jnp.maximum(j - 1, 0) * 8, 8)
            prev_r = jnp.where(j > 0, x_ref[pl.ds(ip, 8), 0:cb], xp_ref[:, 0:cb])
            prev_i = jnp.where(j > 0, x_ref[pl.ds(ip, 8), cb:2 * cb], xp_ref[:, cb:2 * cb])
            edge = jnp.where(jnp.logical_and(j == 0, tblk == nt - 1), 0.0, 1.0)
            xs_r = jnp.where(rows == 0, pltpu.roll(prev_r, 1, 0) * edge, pltpu.roll(x_ref[pl.ds(i, 8), 0:cb], 1, 0))
            xs_i = jnp.where(rows == 0, pltpu.roll(prev_i, 1, 0) * edge, pltpu.roll(x_ref[pl.ds(i, 8), cb:2 * cb], 1, 0))
            sr_acc = sr_acc + lr * xs_r + li * xs_i
            si_acc = si_acc + li * xs_r - lr * xs_i
            return lr[0:1, :], li[0:1, :], sr_acc, si_acc

        cr, ci, sr_acc, si_acc = lax.fori_loop(
            0, nj, step, (carry[0:1, :], carry[1:2, :], acc[:, 0:cb], acc[:, cb:2 * cb]))
        carry[0:1, :] = cr
        carry[1:2, :] = ci
        acc[:, 0:cb] = sr_acc
        acc[:, cb:2 * cb] = si_acc

        @pl.when(tblk == nt - 1)
        def _():
            da_ref[...] = jnp.sum(acc[...], axis=0, keepdims=True)

    blk = pl.BlockSpec((tb, 2 * cb), lambda j, i: (nt - 1 - i, j))
    prev = pl.BlockSpec((8, 2 * cb), lambda j, i: (jnp.maximum((nt - 1 - i) * (tb // 8) - 1, 0), j))
    vec = pl.BlockSpec((1, 2 * cb), lambda j, i: (0, j))
    return _call(
        body, name=name, grid=(SSM_CH // cb, nt), in_specs=[blk, blk, prev, vec], out_specs=[blk, vec],
        out_shape=[jax.ShapeDtypeStruct((t, 2 * SSM_CH), F32), jax.ShapeDtypeStruct((1, 2 * SSM_CH), F32)],
        scratch_shapes=[pltpu.VMEM((8, cb), F32), pltpu.VMEM((8, 2 * cb), F32)],
        sem=("parallel", "arbitrary"), args=(dx, x, x, a), comm=comm)


def _glu_fwd(yc, u, dvec, wg, bg, name, tr=512):
    t = yc.shape[0]

    def body(yc_ref, u_ref, d_ref, w_ref, b_ref, yl_ref, yb_ref):
        yl = yc_ref[...] + d_ref[...] * u_ref[...]
        yl_ref[...] = yl
        yg, _ = _gelu_and_grad(yl)
        z = jnp.dot(yg.astype(BF16), w_ref[...], preferred_element_type=F32) + b_ref[...]
        yb_ref[...] = (yg * _sigmoid(z)).astype(BF16)

    blk = pl.BlockSpec((tr, SSM_WIDTH), lambda i: (i, 0))
    vec = pl.BlockSpec((1, SSM_WIDTH), lambda i: (0, 0))
    return pl.pallas_call(
        body, name=name, grid=(t // tr,),
        in_specs=[blk, blk, vec, pl.BlockSpec((SSM_WIDTH, SSM_WIDTH), lambda i: (0, 0)), vec],
        out_specs=[blk, blk],
        out_shape=[jax.ShapeDtypeStruct((t, SSM_WIDTH), F32), jax.ShapeDtypeStruct((t, SSM_WIDTH), BF16)],
        compiler_params=_params(("parallel",)))(yc, u, dvec, wg, bg)


def _glu_bwd(yl, u, dvec, wg, bg, dyb, name, tr=512):
    t = yl.shape[0]

    def body(yl_ref, u_ref, d_ref, w_ref, b_ref, dy_ref, dyl_ref, du_ref, dw_ref, db_ref, dd_ref):
        @pl.when(pl.program_id(0) == 0)
        def _():
            dw_ref[...] = jnp.zeros_like(dw_ref)
            db_ref[...] = jnp.zeros_like(db_ref)
            dd_ref[...] = jnp.zeros_like(dd_ref)

        ylv, dyv, wv = yl_ref[...], dy_ref[...].astype(F32), w_ref[...]
        yg, dgelu = _gelu_and_grad(ylv)
        ygb = yg.astype(BF16)
        z = jnp.dot(ygb, wv, preferred_element_type=F32) + b_ref[...]
        sg = _sigmoid(z)
        dz = dyv * yg * sg * (1.0 - sg)
        dzb = dz.astype(BF16)
        dyg = dyv * sg + lax.dot_general(dzb, wv, (((1,), (1,)), ((), ())), preferred_element_type=F32)
        dyl = dyg * dgelu
        dyl_ref[...] = dyl.astype(BF16)
        du_ref[...] = dyl * d_ref[...]
        dw_ref[...] += lax.dot_general(ygb, dzb, (((0,), (0,)), ((), ())), preferred_element_type=F32)
        db_ref[...] += jnp.sum(dz, axis=0, keepdims=True)
        dd_ref[...] += jnp.sum(dyl * u_ref[...], axis=0, keepdims=True)

    blk = pl.BlockSpec((tr, SSM_WIDTH), lambda i: (i, 0))
    vec = pl.BlockSpec((1, SSM_WIDTH), lambda i: (0, 0))
    wsp = pl.BlockSpec((SSM_WIDTH, SSM_WIDTH), lambda i: (0, 0))
    return pl.pallas_call(
        body, name=name, grid=(t // tr,), in_specs=[blk, blk, vec, wsp, vec, blk],
        out_specs=[blk, blk, wsp, vec, vec],
        out_shape=[jax.ShapeDtypeStruct((t, SSM_WIDTH), BF16), jax.ShapeDtypeStruct((t, SSM_WIDTH), F32),
                   jax.ShapeDtypeStruct((SSM_WIDTH, SSM_WIDTH), F32), jax.ShapeDtypeStruct((1, SSM_WIDTH), F32),
                   jax.ShapeDtypeStruct((1, SSM_WIDTH), F32)],
        compiler_params=_params(("arbitrary",)))(yl, u, dvec, wg, bg, dyb)


def _mesh_pos():
    return lax.axis_index("x"), lax.axis_index("y"), lax.axis_index("c")


def _device_index():
    x, y, c = _mesh_pos()
    return 4 * x + 2 * y + c


def _gather_comm(arrays):
    na = len(arrays)

    def own_copy(ins, outs, sems, ai):
        return pltpu.make_async_copy(ins[ai], outs[ai].at[_device_index()], sems[2].at[ai])

    def ctx(ins, outs, sems):
        send_sems, recv_sems = sems[:2]
        x, y, c = _mesh_pos()
        chips = [(1 - x, y), (x, 1 - y), (1 - x, 1 - y)]

        def copy(ai, kk, block, to, own=False):
            slot = outs[ai].at[4 * block[0] + 2 * block[1] + block[2]]
            return pltpu.make_async_remote_copy(
                src_ref=ins[ai] if own else slot, dst_ref=slot, send_sem=send_sems.at[ai, kk],
                recv_sem=recv_sems.at[ai, kk], device_id=to, device_id_type=MESH)

        return (x, y, c), (x, y, 1 - c), chips, c, copy

    def start(ins, outs, sems):
        me, sibling, chips, c, copy = ctx(ins, outs, sems)
        for ai in range(na):
            copy(ai, 0, me, sibling, own=True).start()
            for j, chip in enumerate(chips):
                copy(ai, 1 + j, me, (*chip, c), own=True).start()
        for ai in range(na):
            own_copy(ins, outs, sems, ai).start()

    def mid(ins, outs, sems):
        me, sibling, chips, c, copy = ctx(ins, outs, sems)
        for ai in range(na):
            for j, chip in enumerate(chips):
                copy(ai, 1 + j, (*chip, c), me).wait_recv()
                copy(ai, 4 + j, (*chip, c), sibling).start()

    def end(ins, outs, sems):
        me, sibling, chips, c, copy = ctx(ins, outs, sems)
        for ai in range(na):
            copy(ai, 0, sibling, me).wait_recv()
            copy(ai, 0, me, sibling, own=True).wait_send()
            for j, chip in enumerate(chips):
                copy(ai, 4 + j, (*chip, 1 - c), me).wait_recv()
                copy(ai, 1 + j, me, (*chip, c), own=True).wait_send()
                copy(ai, 4 + j, (*chip, c), sibling).wait_send()
            own_copy(ins, outs, sems, ai).wait()

    return Comm(arrays, [jax.ShapeDtypeStruct((N_DEV,) + a.shape, a.dtype) for a in arrays],
                [pltpu.SemaphoreType.DMA((na, 7)), pltpu.SemaphoreType.DMA((na, 7)), pltpu.SemaphoreType.DMA((na,))],
                start, end, mid)


def _sequencer_gather(arrays, name, collective_id):
    comm = _gather_comm(arrays)
    na = len(arrays)

    def body(*refs):
        ins, outs, sems = refs[:na], refs[na:2 * na], refs[2 * na:]
        x, y, c = _mesh_pos()
        peers = [(x, y, 1 - c), (1 - x, y, c), (x, 1 - y, c), (1 - x, 1 - y, c)]
        barrier = pltpu.get_barrier_semaphore()
        for peer in peers:
            pl.semaphore_signal(barrier, inc=1, device_id=peer, device_id_type=MESH)
        pl.semaphore_wait(barrier, len(peers))
        comm.start(ins, outs, sems)
        comm.mid(ins, outs, sems)
        comm.end(ins, outs, sems)

    return list(pl.kernel(
        body, out_type=tuple(comm.out_shapes), mesh=plsc.ScalarSubcoreMesh(axis_name="sequencer", num_cores=1),
        name=name, scratch_types=tuple(comm.sems),
        compiler_params=pltpu.CompilerParams(collective_id=collective_id))(*arrays))


def _sequencer_exchange(comm, peers_of, name, collective_id):
    na = len(comm.inputs)

    def body(*refs):
        ins, outs, sems = refs[:na], refs[na:na + len(comm.out_shapes)], refs[na + len(comm.out_shapes):]
        peers = peers_of(*_mesh_pos())
        barrier = pltpu.get_barrier_semaphore()
        for peer in peers:
            pl.semaphore_signal(barrier, inc=1, device_id=peer, device_id_type=MESH)
        pl.semaphore_wait(barrier, len(peers))
        comm.start(ins, outs, sems)
        comm.end(ins, outs, sems)

    return list(pl.kernel(
        body, out_type=tuple(comm.out_shapes), mesh=plsc.ScalarSubcoreMesh(axis_name="sequencer", num_cores=1),
        name=name, scratch_types=tuple(comm.sems),
        compiler_params=pltpu.CompilerParams(collective_id=collective_id))(*comm.inputs))


SIBLING_SWAP_ID, CHIP_EXCHANGE_ID = 9, 10


def _sequencer_swap(arrays, name):
    return _sequencer_exchange(_swap_comm(arrays), lambda x, y, c: [(x, y, 1 - c)], name, SIBLING_SWAP_ID)[0]


def _sequencer_chips(send, name):
    return _sequencer_exchange(_chips_comm(send), lambda x, y, c: [(1 - x, y, c), (x, 1 - y, c), (1 - x, 1 - y, c)],
                               name, CHIP_EXCHANGE_ID)[0]


def _swap_comm(arrays):
    na = len(arrays)
    offs = np.concatenate([[0], np.cumsum([a.shape[1] for a in arrays])]).astype(int)

    def copies(ins, outs, sems):
        x, y, c = _mesh_pos()
        return [pltpu.make_async_remote_copy(
            src_ref=ins[ai].at[2 * k + 1 - c], dst_ref=outs[0].at[k, pl.ds(int(offs[ai]), arrays[ai].shape[1])],
            send_sem=sems[0].at[ai, k], recv_sem=sems[1].at[ai, k], device_id=(x, y, 1 - c), device_id_type=MESH)
            for ai in range(na) for k in range(4)]

    def start(ins, outs, sems):
        for cp in copies(ins, outs, sems):
            cp.start()

    def end(ins, outs, sems):
        for cp in copies(ins, outs, sems):
            cp.wait()

    return Comm(arrays, [jax.ShapeDtypeStruct((4, int(offs[-1]), PACK_COLS), arrays[0].dtype)],
                [pltpu.SemaphoreType.DMA((na, 4)), pltpu.SemaphoreType.DMA((na, 4))], start, end)


def _chips_comm(send):
    def copies(ins, outs, sems):
        x, y, c = _mesh_pos()
        chips = [(1 - x, y), (x, 1 - y), (1 - x, 1 - y)]
        return [pltpu.make_async_remote_copy(
            src_ref=ins[0].at[2 * cx + cy], dst_ref=outs[0].at[j], send_sem=sems[0].at[j], recv_sem=sems[1].at[j],
            device_id=(cx, cy, c), device_id_type=MESH) for j, (cx, cy) in enumerate(chips)]

    def start(ins, outs, sems):
        for cp in copies(ins, outs, sems):
            cp.start()

    def end(ins, outs, sems):
        for cp in copies(ins, outs, sems):
            cp.wait()

    return Comm([send], [jax.ShapeDtypeStruct((3,) + send.shape[1:], send.dtype)],
                [pltpu.SemaphoreType.DMA((3,)), pltpu.SemaphoreType.DMA((3,))], start, end)


def _pair_sum(keep, recv, name, tr=464):
    nchip, rows, cols = keep.shape

    def body(g_ref, r_ref, o_ref):
        o_ref[...] = (g_ref[...].astype(F32) + r_ref[...].astype(F32)).astype(BF16)

    blk = pl.BlockSpec((1, tr, cols), lambda k, i: (k, i, 0))
    return pl.pallas_call(
        body, name=name, grid=(nchip, rows // tr), in_specs=[blk, blk], out_specs=blk,
        out_shape=jax.ShapeDtypeStruct((nchip, rows, cols), BF16),
        compiler_params=_params(("parallel", "parallel")))(keep, recv)


def _pair_sum_pieces(pieces, recv, name, tr):
    _, rows, cols = pieces.shape
    core = lax.axis_index("c").astype(jnp.int32).reshape(1)

    def body(c_ref, g_ref, r_ref, o_ref):
        del c_ref
        o_ref[...] = (g_ref[...].astype(F32) + r_ref[...].astype(F32)).astype(BF16)

    grid_spec = pltpu.PrefetchScalarGridSpec(
        num_scalar_prefetch=1, grid=(4, rows // tr),
        in_specs=[pl.BlockSpec((1, tr, cols), lambda k, i, c_ref: (2 * k + c_ref[0], i, 0)),
                  pl.BlockSpec((1, tr, cols), lambda k, i, c_ref: (k, i, 0))],
        out_specs=pl.BlockSpec((1, tr, cols), lambda k, i, c_ref: (k, i, 0)))
    return pl.pallas_call(
        body, name=name, grid_spec=grid_spec, out_shape=jax.ShapeDtypeStruct((4, rows, cols), BF16),
        compiler_params=_params(("parallel", "parallel")))(core, pieces, recv)


def _chip_sum(own, others, name, tr=464):
    _, rows, cols = own.shape
    chip = (2 * lax.axis_index("x") + lax.axis_index("y")).astype(jnp.int32).reshape(1)

    def body(chip_ref, own_ref, oth_ref, o_ref):
        del chip_ref
        acc = own_ref[0].astype(F32)
        for j in range(3):
            acc = acc + oth_ref[j].astype(F32)
        o_ref[...] = acc

    grid_spec = pltpu.PrefetchScalarGridSpec(
        num_scalar_prefetch=1, grid=(rows // tr,),
        in_specs=[pl.BlockSpec((1, tr, cols), lambda i, chip_ref: (chip_ref[0], i, 0)),
                  pl.BlockSpec((3, tr, cols), lambda i, chip_ref: (0, i, 0))],
        out_specs=pl.BlockSpec((tr, cols), lambda i, chip_ref: (i, 0)))
    return pl.pallas_call(
        body, name=name, grid_spec=grid_spec, out_shape=jax.ShapeDtypeStruct((rows, cols), F32),
        compiler_params=_params(("parallel",)))(chip, own, others)


def _sum_leading(parts, name, tr=464):
    nparts, rows, cols = parts.shape
    tr = tr if rows % tr == 0 else rows

    def body(p_ref, o_ref):
        acc = p_ref[0].astype(F32)
        for i in range(1, nparts):
            acc = acc + p_ref[i].astype(F32)
        o_ref[...] = acc

    return pl.pallas_call(
        body, name=name, grid=(rows // tr,),
        in_specs=[pl.BlockSpec((nparts, tr, cols), lambda i: (0, i, 0))],
        out_specs=pl.BlockSpec((tr, cols), lambda i: (i, 0)), out_shape=jax.ShapeDtypeStruct((rows, cols), F32),
        compiler_params=_params(("parallel",)))(parts)


def _adamw(w, g, m, v, name, comm=None):
    shape = w.shape
    cols = shape[-1]
    lead = shape[0] if len(shape) >= 3 else 1
    rows = int(np.prod(shape[:-1])) // lead if len(shape) > 1 else 1
    w2, g2, m2, v2 = (a.reshape(lead, rows, cols) for a in (w, g, m, v))
    tr = max([t for t in range(8, min(rows, 512) + 1, 8) if rows % t == 0], default=rows)
    bc1, bc2 = 1.0 - ADAM_B1 ** ADAM_STEP, 1.0 - ADAM_B2 ** ADAM_STEP

    def body(w_ref, g_ref, m_ref, v_ref, d_ref, nm_ref, nv_ref):
        gv = g_ref[...]
        nm = ADAM_B1 * m_ref[...] + (1.0 - ADAM_B1) * gv
        nv = ADAM_B2 * v_ref[...] + (1.0 - ADAM_B2) * (gv * gv)
        nm_ref[...] = nm
        nv_ref[...] = nv
        d_ref[...] = -ADAM_LR * ((nm / bc1) / (jnp.sqrt(nv / bc2) + ADAM_EPS) + ADAM_WD * w_ref[...])

    blk = pl.BlockSpec((1, tr, cols), lambda l, i: (l, i, 0))
    res = _call(body, name=name, grid=(lead, rows // tr), in_specs=[blk] * 4, out_specs=[blk] * 3,
                out_shape=[jax.ShapeDtypeStruct((lead, rows, cols), F32)] * 3, sem=("parallel", "parallel"),
                args=(w2, g2, m2, v2), comm=comm)
    outs, couts = res if comm is not None else (res, None)
    outs = tuple(o.reshape(shape) for o in outs)
    return outs if comm is None else (outs, couts)


WEIGHT_NAMES = ['norm_mix_g', 'norm_xa_g', 'norm_ffn_g', 'norm_mem_g', 'norm_final_g', 'w_in_ab', 'conv_qkv_a',
                'a_log_a', 'dt_bias_a', 'onorm_g_a', 'ssm_lambda_re', 'ssm_lambda_im', 'ssm_b_re', 'ssm_b_im',
                'ssm_c_re', 'ssm_c_im', 'ssm_d', 'ssm_log_dt', 'w_glu_b', 'b_glu_b', 'w_out_ab', 'pool_w',
                'pool_scale', 'xa_wq', 'xa_wkv', 'xa_wo', 'ffn_w_up', 'ffn_conv', 'ffn_w_down']
BIG_SHARDED = {'w_in_ab': ((1, 1024, 2568), 2), 'w_glu_b': ((1, 512, 512), 1), 'w_out_ab': ((1, 1024, 1024), 1),
               'pool_w': ((1, 4, 256, 256), 2), 'xa_wq': ((2, 1024, 1024), 1), 'xa_wkv': ((2, 1024, 2048), 2),
               'xa_wo': ((2, 1024, 1024), 1), 'ffn_w_up': ((2, 1024, 5632), 2), 'ffn_w_down': ((2, 2816, 1024), 1)}
SMALL_SHARDED = {'conv_qkv_a': ((1, 4, 1536), 2), 'pool_scale': ((1, 1024), 1), 'ffn_conv': ((2, 3, 5632), 2)}
REPLICATED = {'norm_mix_g': (2, 1024), 'norm_xa_g': (2, 1024), 'norm_ffn_g': (2, 1024), 'norm_mem_g': (1024,),
              'norm_final_g': (1024,), 'a_log_a': (1, 4), 'dt_bias_a': (1, 4), 'onorm_g_a': (1, 128),
              'ssm_lambda_re': (1, 32, 64), 'ssm_lambda_im': (1, 32, 64), 'ssm_b_re': (1, 32, 64, 16),
              'ssm_b_im': (1, 32, 64, 16), 'ssm_c_re': (1, 32, 16, 64), 'ssm_c_im': (1, 32, 16, 64),
              'ssm_d': (1, 32, 16), 'ssm_log_dt': (1, 32), 'b_glu_b': (1, 512)}
PACK_ROW_ALIGN = 8


def _shard_shape(shape, axis):
    return tuple(s // N_DEV if i == axis else s for i, s in enumerate(shape))


def _round_up(n, m):
    return (n + m - 1) // m * m


def _pack(arrays):
    total = sum(int(np.prod(a.shape)) for a in arrays)
    padded = _round_up(total, PACK_COLS * PACK_ROW_ALIGN)
    parts = [a.astype(F32).reshape(-1) for a in arrays]
    if padded != total:
        parts.append(jnp.zeros((padded - total,), F32))
    return jnp.concatenate(parts).reshape(padded // PACK_COLS, PACK_COLS)


def _unpack(packed, shapes):
    flat, out, off = packed.reshape(-1), [], 0
    for shape in shapes:
        size = int(np.prod(shape))
        out.append(flat[off:off + size].reshape(shape))
        off += size
    return out


def _split_shards(full, axis):
    shape = full.shape
    s = shape[axis] // N_DEV
    a = full.reshape(shape[:axis] + (N_DEV, s) + shape[axis + 1:])
    return jnp.moveaxis(a, axis, 0).reshape(N_DEV, -1)


def _merge_shards(pieces, shape, axis):
    sh = _shard_shape(shape, axis)
    a = pieces.reshape((N_DEV,) + sh)
    a = jnp.moveaxis(a, 0, axis)
    return a.reshape(shape)


_SCAN_NB = SSM_CH // SCAN_CB


def _to_scan_layout(m, axis):
    shape = m.shape
    m = m.reshape(shape[:axis] + (2, _SCAN_NB, SCAN_CB) + shape[axis + 1:])
    return jnp.swapaxes(m, axis, axis + 1).reshape(shape)


def _from_scan_layout(m, axis):
    shape = m.shape
    m = m.reshape(shape[:axis] + (_SCAN_NB, 2, SCAN_CB) + shape[axis + 1:])
    return jnp.swapaxes(m, axis, axis + 1).reshape(shape)


def _s5_discretise(lam_re, lam_im, b_re, b_im, log_dt):
    dt = jnp.exp(log_dt)[:, None]
    mag = jnp.exp(lam_re * dt)
    ang = lam_im * dt
    lb_re, lb_im = mag * jnp.cos(ang), mag * jnp.sin(ang)
    den = lam_re * lam_re + lam_im * lam_im
    nr, ni = lb_re - 1.0, lb_im
    coef_re = (nr * lam_re + ni * lam_im) / den
    coef_im = (ni * lam_re - nr * lam_im) / den
    bb_re = coef_re[..., None] * b_re - coef_im[..., None] * b_im
    bb_im = coef_re[..., None] * b_im + coef_im[..., None] * b_re
    return lb_re, lb_im, bb_re, bb_im


_GROUPS_PER_BLOCK = N_GROUPS // _SCAN_NB
_U_BLOCK = _GROUPS_PER_BLOCK * SSM_GROUP


def _s5_matrices(lb_re, lb_im, bb_re, bb_im, c_re, c_im):
    eye = jnp.eye(_GROUPS_PER_BLOCK, dtype=F32)
    blocked = lambda m: m.reshape((_SCAN_NB, _GROUPS_PER_BLOCK) + m.shape[1:])
    bmat = lambda bb: jnp.einsum('jgph,gk->jghkp', blocked(bb), eye).reshape(_SCAN_NB, _U_BLOCK, SCAN_CB)
    cmat = lambda cc: jnp.einsum('jghp,gk->jkpgh', blocked(cc), eye).reshape(_SCAN_NB, SCAN_CB, _U_BLOCK)
    b_in = jnp.concatenate([bmat(bb_re), bmat(bb_im)], axis=2)
    c_out = jnp.concatenate([cmat(c_re), -cmat(c_im)], axis=1)
    a_row = _to_scan_layout(jnp.concatenate([lb_re.reshape(1, SSM_CH), lb_im.reshape(1, SSM_CH)], axis=1), 1)
    return b_in, c_out, a_row


def _s5_matrix_grads(db_in, dc_out, da_row):
    da_nat = _from_scan_layout(da_row, 1)
    eye = jnp.eye(_GROUPS_PER_BLOCK, dtype=F32)
    nb, gb = _SCAN_NB, _GROUPS_PER_BLOCK
    bgrad = lambda m: jnp.einsum('jghkp,gk->jgph', m.reshape(nb, gb, SSM_GROUP, gb, SSM_STATE), eye
                                 ).reshape(N_GROUPS, SSM_STATE, SSM_GROUP)
    cgrad = lambda m: jnp.einsum('jkpgh,gk->jghp', m.reshape(nb, gb, SSM_STATE, gb, SSM_GROUP), eye
                                 ).reshape(N_GROUPS, SSM_GROUP, SSM_STATE)
    dbb_re, dbb_im = bgrad(db_in[:, :, :SCAN_CB]), bgrad(db_in[:, :, SCAN_CB:])
    dc_re, dc_im = cgrad(dc_out[:, :SCAN_CB]), -cgrad(dc_out[:, SCAN_CB:])
    dlb_re = da_nat[0, :SSM_CH].reshape(N_GROUPS, SSM_STATE)
    dlb_im = da_nat[0, SSM_CH:].reshape(N_GROUPS, SSM_STATE)
    return dlb_re, dlb_im, dbb_re, dbb_im, dc_re, dc_im


def _as_pieces(a):
    return a.reshape(N_DEV, a.shape[0] // N_DEV, a.shape[1])


def _hybrid_fwd(xn, x, wts, p, weights, riders):
    sv = {}
    hq = _mm(xn, wts['w_qkv_t'], "nt", "l0_in_qkv")
    gate = _mm(xn, wts['w_gate_t'], "nt", "l0_in_gate")
    ba = _mm(xn, wts['w_ba_t'], "nt", "l0_in_ba")
    u = _mm(xn, wts['w_u_t'], "nt", "l0_in_u")
    conv = p['conv_qkv']
    qkv = _qkv_pre_fwd(hq, conv, "l0_qkv_pre")
    gates = _gates_fwd(ba, p['arow'], p['brow'], "l0_gates")
    o, tm_all, s_all = riders.run("l0_gdr_fwd", _gdr_fwd, qkv, gates)
    wts['w_glu'], wts['w_out'] = weights.full['w_glu'], weights.full['w_out']
    y_a = _onorm_fwd(o, gate, p['onorm_g'], "l0_onorm")
    bu = riders.run("l0_s5_bu", _mm_bd, u, p['b_in'], "nn")
    xs = riders.run("l0_s5_scan", _s5_scan_fwd, bu, p['a_row'])
    weights.gather_by_sequencer(GATHER_LAYER1, xs, "gather_layer1", GATHER_LAYER1_ID)
    yc = riders.run("l0_s5_cx", _mm_bd, xs, p['c_out'], "nn")
    yl, y_b = _glu_fwd(yc, u, p['d_row'], wts['w_glu'], p['b_glu'], "l0_glu")
    mixed = jnp.concatenate([y_a, y_b], axis=1)
    x1 = _mm(mixed, wts['w_out'], "nn", "l0_out", res=x)
    sv.update(hq=hq, gate=gate, ba=ba, u=u, qkv=qkv, gb=gates, o=o, tm=tm_all, s=s_all, xs=xs, yl=yl, mixed=mixed)
    return x1, sv


def _hybrid_bwd(dx1, xn, wts, p, sv, riders):
    gr = {}
    dmixed = _mm(dx1, wts['w_out'], "nt", "l0_out_dx", out_dtype=BF16)
    riders.grad('w_out', _as_pieces(_mm(sv['mixed'], dx1, "tn", "l0_out_dw", out_dtype=BF16)))
    dya, dyb = dmixed[:, :WIDTH_A], dmixed[:, WIDTH_A:]
    dyl, du_direct, dw_glu, gr['b_glu_b'], dd = _glu_bwd(
        sv['yl'], sv['u'], p['d_row'], wts['w_glu'], p['b_glu'], dyb, "l0_glu_bwd")
    riders.grad('w_glu', dw_glu.astype(BF16).reshape(N_DEV, -1, PACK_COLS))
    dxs = riders.run("l0_s5_cx_dx", _mm_bd, dyl, p['c_out'], "nt")
    dc_out = _mm_bd(sv['xs'], dyl, "tn", "l0_s5_cx_dw")
    lam, da_row = riders.run("l0_s5_scan_bwd", _s5_scan_bwd, dxs, sv['xs'], p['a_row'])
    du = _mm_bd(lam, p['b_in'], "nt", "l0_s5_bu_dx", res=du_direct, out_dtype=BF16)
    db_in = _mm_bd(sv['u'], lam, "tn", "l0_s5_bu_dw")
    gr['s5'] = (db_in, dc_out, da_row, dd)
    do, dgate, gr['onorm_g_a'] = _onorm_bwd(sv['o'], sv['gate'], p['onorm_g'], dya, "l0_onorm_bwd")
    dqkv, dgb = riders.run("l0_gdr_bwd", _gdr_bwd, sv['qkv'], sv['gb'], sv['tm'], sv['s'], do)
    dhq, gr['conv_qkv_a'] = _qkv_pre_bwd(sv['hq'], p['conv_qkv'], dqkv, "l0_qkv_pre_bwd")
    dba, da_log, ddt_bias = _gates_bwd(sv['ba'], p['arow'], p['brow'], dgb, "l0_gates_bwd")
    gr['a_log_a'], gr['dt_bias_a'] = da_log[:, 4:8], ddt_bias[:, 4:8]
    dw_qkv_t = _mm(dhq, xn, "tn", "l0_in_qkv_dw", out_dtype=BF16)
    dw_gate_t = _mm(dgate, xn, "tn", "l0_in_gate_dw", out_dtype=BF16)
    dw_ba_t = _mm(dba, xn, "tn", "l0_in_ba_dw", out_dtype=BF16)
    dw_u_t = _mm(du, xn, "tn", "l0_in_u_dw", out_dtype=BF16)
    dw_in_t = _as_pieces(jnp.concatenate([dw_qkv_t, dw_gate_t, dw_ba_t[:8], dw_u_t], axis=0))
    riders.grad('w_in_t', jnp.concatenate(
        [dw_in_t, jnp.zeros((N_DEV, dict(PIECES)['w_in_t'] - W_IN_PIECE, D_MODEL), BF16)], axis=1))
    dxn = riders.run("l0_in_qkv_dx", _mm, dhq, wts['w_qkv_t'], "nn")
    dxn = _mm(dgate, wts['w_gate_t'], "nn", "l0_in_gate_dx", res=dxn)
    dxn = _mm(dba, wts['w_ba_t'], "nn", "l0_in_ba_dx", res=dxn)
    dxn = riders.run("l0_in_u_dx", _mm, du, wts['w_u_t'], "nn", res=dxn)
    return dxn, gr


def _xa_fwd(x1, g, mem_n, wq, wkv_t, wo, tag, riders):
    xq = _rms_fwd(x1, g, BF16, tag + "_norm")
    q = _mm(xq, wq, "nn", tag + "_q", out_dtype=BF16)
    kv = _mm(mem_n, wkv_t, "nt", tag + "_kv", out_dtype=BF16)
    o = riders.run(tag + "_attn", _attn_fwd, q, kv)
    x2 = _mm(o, wo, "nn", tag + "_o", res=x1)
    return x2, dict(xq=xq, q=q, kv=kv, o=o)


def _xa_bwd(dx2, x1, g, mem_n, wq, wkv_t, wo, sv, tag, layer, riders):
    do = _mm(dx2, wo, "nt", tag + "_o_dx", out_dtype=BF16)
    riders.grad('wo%d' % layer, _as_pieces(_mm(sv['o'], dx2, "tn", tag + "_o_dw", out_dtype=BF16)))
    dq, dk, dv = _attn_bwd(sv['q'], sv['kv'], do, tag + "_attn_bwd")
    dkv = jnp.concatenate([dk, dv], axis=1).astype(BF16)
    dxq = _mm(dq, wq, "nt", tag + "_q_dx")
    riders.grad('wq%d' % layer, _as_pieces(_mm(sv['xq'], dq, "tn", tag + "_q_dw", out_dtype=BF16)))
    dmem_n = _mm(dkv, wkv_t, "nn", tag + "_kv_dx")
    riders.grad('wkv_t%d' % layer, _as_pieces(_mm(dkv, mem_n, "tn", tag + "_kv_dw", out_dtype=BF16)))
    dx1, dg = riders.run(tag + "_norm_bwd", _rms_bwd, x1, g, dxq, dx2)
    return dx1, dmem_n, dg


def _ffn_fwd(x2, g, w_up_t, conv, w_down, tag, riders):
    xf = _rms_fwd(x2, g, BF16, tag + "_norm")
    h = riders.run(tag + "_up", _mm, xf, w_up_t, "nt")
    a = riders.run(tag + "_act", _ffn_act_fwd, h, conv)
    x3 = _mm(a, w_down, "nn", tag + "_down", res=x2)
    return x3, dict(xf=xf, h=h, a=a)


def _ffn_bwd(dx3, x2, g, w_up_t, conv, w_down, sv, tag, layer, riders):
    da = _mm(dx3, w_down, "nt", tag + "_down_dx")
    riders.grad('down%d' % layer, _as_pieces(_mm(sv['a'], dx3, "tn", tag + "_down_dw", out_dtype=BF16)))
    dh, dconv = riders.run(tag + "_act_bwd", _ffn_act_bwd, sv['h'], conv, da)
    dxf = riders.run(tag + "_up_dx", _mm_parts, dh, w_up_t, "nn")
    dw_up_t = riders.run(tag + "_up_dw", _mm_parts, dh, sv['xf'], "tn", out_dtype=BF16)
    riders.grad('up_t%d' % layer, _as_pieces(dw_up_t))
    dx2, dg = riders.run(tag + "_norm_bwd", _rms_bwd, x2, g, dxf, dx3)
    return dx2, dconv, dg


BIG_NAMES, SMALL_NAMES, REP_NAMES = list(BIG_SHARDED), list(SMALL_SHARDED), list(REPLICATED)
SMALL_SIZES = [int(np.prod(_shard_shape(*SMALL_SHARDED[n]))) for n in SMALL_NAMES]


PIECES = [('w_in_t', 384), ('w_glu', 32), ('w_out', 128), ('pool_w', 32), ('wq0', 128), ('wq1', 128),
          ('wkv_t0', 256), ('wkv_t1', 256), ('wo0', 128), ('wo1', 128), ('up_t0', 704), ('up_t1', 704),
          ('down0', 352), ('down1', 352)]
W_IN_ROWS = 4 * WIDTH_A + 2 * N_HEADS_A + SSM_WIDTH
W_IN_PIECE = W_IN_ROWS // N_DEV


def _row_tile(rows):
    return max(t for t in range(16, min(rows, 1024) + 1, 16) if rows % t == 0)


class _Riders:
    def __init__(self):
        self.waiting = {}
        self.deferred = {}
        self.grads = {}
        self.groups = []
        self.reduced = {}

    def add(self, host, comm, then):
        self.waiting.setdefault(host, []).append((comm, then))

    def after(self, marker, then):
        self.deferred.setdefault(marker, []).append(then)

    def mark(self, name, out=None):
        for cont in self.deferred.pop(name, []):
            step = cont()
            if step is not None:
                values, then = step
                out, values = lax.optimization_barrier((out, values))
                then(values)
        return out

    def run(self, name, fn, *args, **kw):
        riders = self.waiting.pop(name, [])
        if not riders:
            out = fn(*args, name=name, **kw)
        else:
            out, couts = fn(*args, name=name, comm=[c for c, _ in riders], **kw)
            for (_, then), got in zip(riders, couts):
                then(got)
        return self.mark(name, out)

    def grad(self, key, pieces):
        self.grads[key] = pieces
        for group in [g for g in self.groups if all(k in self.grads for k in g[1])]:
            self.groups.remove(group)
            self._reduce(*group)

    def _reduce(self, name, keys, pair_marker, sum_marker):
        arrays = [self.grads[k] for k in keys]
        rows = sum(a.shape[1] for a in arrays)
        tile = _row_tile(rows)
        from_sibling = _sequencer_swap(arrays, name + "_to_sibling")

        def after_swap():
            if len(arrays) == 1:
                chip_sums = _pair_sum_pieces(arrays[0], from_sibling, name + "_pair_sum", tr=tile)
            else:
                core = lax.axis_index("c")
                keep = jnp.concatenate(
                    [lax.dynamic_index_in_dim(a.reshape(4, 2, a.shape[1], PACK_COLS), core, 1, keepdims=False)
                     for a in arrays], axis=1)
                chip_sums = _pair_sum(keep, from_sibling, name + "_pair_sum", tr=tile)

            def exchange_among_chips(chip_sums):
                from_chips = _sequencer_chips(chip_sums, name + "_to_chips")

                def store(total):
                    off = 0
                    for k, a in zip(keys, arrays):
                        self.reduced[k] = total[off:off + a.shape[1]]
                        off += a.shape[1]

                self.after(sum_marker, lambda: (
                    _chip_sum(chip_sums, from_chips, name + "_chip_sum", tr=tile), store))

            return chip_sums, exchange_among_chips

        self.after(pair_marker, after_swap)


class _Weights:
    def __init__(self, inp):
        bf = lambda a: a.astype(BF16)
        local = {'w_in_t': bf(inp['w_in_ab'][0]).T, 'w_glu': bf(inp['w_glu_b'][0]), 'w_out': bf(inp['w_out_ab'][0]),
                 'pool_w': bf(inp['pool_w'][0]),
                 'small': _pack([inp[n] for n in SMALL_NAMES])}
        for l in range(2):
            local['wq%d' % l] = bf(inp['xa_wq'][l])
            local['wkv_t%d' % l] = bf(inp['xa_wkv'][l]).T
            local['wo%d' % l] = bf(inp['xa_wo'][l])
            local['up_t%d' % l] = bf(inp['ffn_w_up'][l]).T
            local['down%d' % l] = bf(inp['ffn_w_down'][l])
        self.local, self.full = local, {}

    def plan(self, keys):
        return _gather_comm([self.local[k] for k in keys])

    def gather_by_sequencer(self, keys, after, name, collective_id):
        arrays = [self.local[k] for k in keys]
        tie = (after.reshape(-1)[0] * 0.0).astype(arrays[0].dtype)
        arrays[0] = arrays[0] + tie
        self.land(keys, _sequencer_gather(arrays, name, collective_id))

    def land(self, keys, gathered):
        for k, g in zip(keys, gathered):
            if k == 'small':
                off = 0
                for n, size in zip(SMALL_NAMES, SMALL_SIZES):
                    self.full[n] = _merge_shards(g.reshape(N_DEV, -1)[:, off:off + size], *SMALL_SHARDED[n])
                    off += size
            elif k == 'pool_w':
                self.full[k] = jnp.swapaxes(g, 0, 1).reshape(len(POOL_WINDOWS), POOL_GROUP, POOL_GROUP)
            else:
                self.full[k] = g.reshape(N_DEV * g.shape[1], g.shape[2])


GATHER_FIRST = ['w_in_t', 'small']
GATHER_LAYER0 = ['w_glu', 'w_out', 'wq0', 'wkv_t0', 'wo0', 'down0', 'up_t0']
GATHER_LAYER1 = ['pool_w', 'wq1', 'wkv_t1', 'wo1', 'up_t1', 'down1']
GATHER_LAYER0_ID, GATHER_LAYER1_ID = 7, 8
GRAD_RIDES = [('g_down1', ['down1'], 'l1_ffn_up_dx', 'l1_xa_norm_bwd'),
              ('g_up1', ['up_t1'], 'l1_xa_norm_bwd', 'l0_ffn_up_dx'),
              ('g_xa1', ['wq1', 'wkv_t1', 'wo1', 'pool_w'], 'l0_ffn_up_dx', 'l0_xa_norm_bwd'),
              ('g_down0', ['down0'], 'l0_ffn_up_dx', 'l0_s5_scan_bwd'),
              ('g_l0', ['up_t0', 'wq0', 'wkv_t0', 'wo0'], 'l0_s5_cx_dx', 'l0_in_u_dx'),
              ('g_out', ['w_out', 'w_glu'], 'l0_s5_scan_bwd', 'l0_in_u_dx'),
              ('g_in', ['w_in_t'], 'l0_in_u_dx', 'adamw_pool_w')]


def _local_step(inp):
    f32_of = lambda n: inp[n].astype(F32)
    weights = _Weights(inp)
    riders = _Riders()
    riders.groups = list(GRAD_RIDES)
    full = weights.full
    weights.land(GATHER_FIRST, _comm_only(weights.plan(GATHER_FIRST), "gather_first"))
    weights.gather_by_sequencer(GATHER_LAYER0, full['w_in_t'], "gather_layer0", GATHER_LAYER0_ID)
    w_in_t = full['w_in_t']
    wts0 = dict(w_qkv_t=w_in_t[:3 * WIDTH_A], w_gate_t=w_in_t[3 * WIDTH_A:4 * WIDTH_A],
                w_ba_t=jnp.concatenate([w_in_t[4 * WIDTH_A:4 * WIDTH_A + 8], jnp.zeros((LANE - 8, D_MODEL), BF16)], 0),
                w_u_t=w_in_t[4 * WIDTH_A + 8:])
    lb_disc, disc_vjp = jax.vjp(_s5_discretise, f32_of('ssm_lambda_re')[0], f32_of('ssm_lambda_im')[0],
                                f32_of('ssm_b_re')[0], f32_of('ssm_b_im')[0], f32_of('ssm_log_dt')[0])
    b_in, c_out, a_row = _s5_matrices(*lb_disc, f32_of('ssm_c_re')[0], f32_of('ssm_c_im')[0])
    zeros4 = jnp.zeros((1, 4), F32)
    p0 = dict(conv_qkv=full['conv_qkv_a'][0], onorm_g=f32_of('onorm_g_a'),
              arow=jnp.concatenate([zeros4, f32_of('a_log_a'), jnp.zeros((1, LANE - 8), F32)], 1),
              brow=jnp.concatenate([zeros4, f32_of('dt_bias_a'), jnp.zeros((1, LANE - 8), F32)], 1),
              b_in=b_in.astype(BF16), c_out=c_out.astype(BF16), a_row=a_row,
              d_row=f32_of('ssm_d').reshape(1, SSM_WIDTH), b_glu=f32_of('b_glu_b'))

    x0 = inp['x'][0]
    mem_n = _rms_fwd(inp['mem'][0], inp['norm_mem_g'], BF16, "mem_norm")
    xn0 = _rms_fwd(x0, inp['norm_mix_g'][0], BF16, "l0_mix_norm")
    x1, sv_mix0 = _hybrid_fwd(xn0, x0, wts0, p0, weights, riders)
    x2, sv_xa0 = _xa_fwd(x1, inp['norm_xa_g'][0], mem_n, full['wq0'], full['wkv_t0'], full['wo0'], "l0_xa", riders)
    x3, sv_ffn0 = _ffn_fwd(x2, inp['norm_ffn_g'][0], full['up_t0'], full['ffn_conv'][0], full['down0'], "l0_ffn", riders)
    xn1 = _rms_fwd(x3, inp['norm_mix_g'][1], F32, "l1_mix_norm")
    x4 = _pool_fwd(xn1, full['pool_w'], full['pool_scale'], x3, "l1_pool")
    x5, sv_xa1 = _xa_fwd(x4, inp['norm_xa_g'][1], mem_n, full['wq1'], full['wkv_t1'], full['wo1'], "l1_xa", riders)
    x6, sv_ffn1 = _ffn_fwd(x5, inp['norm_ffn_g'][1], full['up_t1'], full['ffn_conv'][1], full['down1'], "l1_ffn", riders)
    loss_part, dx6, dg_final = _loss_head(x6, inp['norm_final_g'], inp['loss_target'][0], "loss_head")

    dx5, dconv1, dg_ffn1 = _ffn_bwd(dx6, x5, inp['norm_ffn_g'][1], full['up_t1'], full['ffn_conv'][1], full['down1'],
                                    sv_ffn1, "l1_ffn", 1, riders)
    dx4, dmem1, dg_xa1 = _xa_bwd(dx5, x4, inp['norm_xa_g'][1], mem_n, full['wq1'], full['wkv_t1'], full['wo1'],
                                 sv_xa1, "l1_xa", 1, riders)
    dxn1, dpool_w, dpool_scale = _pool_bwd(xn1, full['pool_w'], full['pool_scale'], dx4, "l1_pool_bwd")
    pool_pieces = jnp.swapaxes(dpool_w.astype(BF16).reshape(len(POOL_WINDOWS), N_DEV, -1, POOL_GROUP), 0, 1)
    riders.grad('pool_w', pool_pieces.reshape(N_DEV, -1, PACK_COLS))
    dx3, dg_mix1 = riders.run("l1_mix_norm_bwd", _rms_bwd, x3, inp['norm_mix_g'][1], dxn1, dx4)
    dx2, dconv0, dg_ffn0 = _ffn_bwd(dx3, x2, inp['norm_ffn_g'][0], full['up_t0'], full['ffn_conv'][0], full['down0'],
                                    sv_ffn0, "l0_ffn", 0, riders)
    dx1, dmem0, dg_xa0 = _xa_bwd(dx2, x1, inp['norm_xa_g'][0], mem_n, full['wq0'], full['wkv_t0'], full['wo0'],
                                 sv_xa0, "l0_xa", 0, riders)
    dxn0, g_mix0 = _hybrid_bwd(dx1, xn0, wts0, p0, sv_mix0, riders)
    grad_x, dg_mix0 = _rms_bwd(x0, inp['norm_mix_g'][0], dxn0, dx1, "l0_mix_norm_bwd")
    _, dg_mem = _rms_bwd(inp['mem'][0], inp['norm_mem_g'], dmem0 + dmem1, None, "mem_norm_bwd")
    assert not riders.groups and not riders.waiting and all(k.startswith("adamw_") for k in riders.deferred), (
        riders.groups, list(riders.waiting), list(riders.deferred))

    db_in, dc_out, da_row, dd = g_mix0['s5']
    dlb_re, dlb_im, dbb_re, dbb_im, dc_re, dc_im = _s5_matrix_grads(db_in, dc_out, da_row)
    dlam_re, dlam_im, dbr, dbi, dlog_dt = disc_vjp((dlb_re, dlb_im, dbb_re, dbb_im))

    rep_grads = {
        'norm_mix_g': jnp.concatenate([dg_mix0, dg_mix1], 0), 'norm_xa_g': jnp.concatenate([dg_xa0, dg_xa1], 0),
        'norm_ffn_g': jnp.concatenate([dg_ffn0, dg_ffn1], 0), 'norm_mem_g': dg_mem.reshape(-1),
        'norm_final_g': dg_final.reshape(-1), 'a_log_a': g_mix0['a_log_a'], 'dt_bias_a': g_mix0['dt_bias_a'],
        'onorm_g_a': g_mix0['onorm_g_a'], 'ssm_lambda_re': dlam_re[None], 'ssm_lambda_im': dlam_im[None],
        'ssm_b_re': dbr[None], 'ssm_b_im': dbi[None], 'ssm_c_re': dc_re[None], 'ssm_c_im': dc_im[None],
        'ssm_d': dd.reshape(1, N_GROUPS, SSM_GROUP), 'ssm_log_dt': dlog_dt[None], 'b_glu_b': g_mix0['b_glu_b']}
    small_grads = {'conv_qkv_a': g_mix0['conv_qkv_a'][None], 'pool_scale': dpool_scale,
                   'ffn_conv': jnp.stack([dconv0, dconv1])}
    return loss_part, grad_x, riders, rep_grads, small_grads


ADAMW_ORDER = ['ffn_w_up', 'ffn_w_down', 'xa_wkv', 'xa_wq', 'xa_wo', 'w_out_ab', 'w_glu_b', 'pool_w', 'w_in_ab']


def _update(inp, loss_part, grad_x, riders, rep_grads, small_grads):
    dev = _device_index()
    misc_local = _pack([rep_grads[n] for n in REP_NAMES] + [small_grads[n] for n in SMALL_NAMES] + [loss_part])
    (misc_all,) = _sequencer_gather([misc_local], "gather_small_grads", GATHER_LAYER0_ID)
    piece = lambda key: riders.reduced[key]
    both = lambda name: jnp.stack([piece(name + '0'), piece(name + '1')])
    swap = lambda a: jnp.swapaxes(a, -1, -2)
    reduced = {'w_in_ab': lambda: piece('w_in_t')[:W_IN_PIECE][None],
               'w_glu_b': lambda: piece('w_glu').reshape(inp['w_glu_b'].shape),
               'w_out_ab': lambda: piece('w_out')[None], 'pool_w': lambda: piece('pool_w').reshape(inp['pool_w'].shape),
               'xa_wq': lambda: both('wq'), 'xa_wkv': lambda: both('wkv_t'), 'xa_wo': lambda: both('wo'),
               'ffn_w_up': lambda: both('up_t'), 'ffn_w_down': lambda: both('down')}
    transposed = ('w_in_ab', 'xa_wkv', 'ffn_w_up')
    grads, upd = {}, {}
    assert sorted(ADAMW_ORDER) == sorted(BIG_NAMES)
    for n in ADAMW_ORDER:
        fix = swap if n in transposed else (lambda a: a)
        g = reduced[n]()
        out = riders.run("adamw_" + n, _adamw, fix(inp[n]), g, fix(inp['m_' + n]), fix(inp['v_' + n]))
        upd[n], grads[n] = tuple(fix(o) for o in out), fix(g)
    assert not riders.waiting and not riders.deferred, (list(riders.waiting), list(riders.deferred))
    misc_sum = _sum_leading(misc_all, "small_grads_sum")
    misc = _unpack(misc_sum, [inp[n].shape for n in REP_NAMES] + [SMALL_SHARDED[n][0] for n in SMALL_NAMES] + [()])
    loss = misc.pop()
    for n, g in zip(REP_NAMES, misc):
        grads[n] = g
    for n, g in zip(SMALL_NAMES, misc[len(REP_NAMES):]):
        grads[n] = lax.dynamic_index_in_dim(_split_shards(g, SMALL_SHARDED[n][1]), dev, 0, keepdims=False
                                            ).reshape(inp[n].shape)
    tiny_names = REP_NAMES + SMALL_NAMES
    rep_total = sum(int(np.prod(inp[n].shape)) for n in REP_NAMES)
    packs = [_pack([inp[prefix + n] for n in tiny_names]) for prefix in ('', 'm_', 'v_')]
    g_pack = _pack([misc_sum.reshape(-1)[:rep_total]] + [grads[n] for n in SMALL_NAMES])
    tiny_out = [_unpack(o, [inp[n].shape for n in tiny_names])
                for o in _adamw(packs[0], g_pack, packs[1], packs[2], "adamw_small")]
    for i, n in enumerate(tiny_names):
        upd[n] = tuple(o[i] for o in tiny_out)

    outs = [loss, grad_x[None]]
    outs += [grads[n] for n in WEIGHT_NAMES]
    for i in range(3):
        outs += [upd[n][i] for n in WEIGHT_NAMES]
    return tuple(outs)


def _step(inp):
    loss_part, grad_x, riders, rep_grads, small_grads = _local_step(inp)
    return _update(inp, loss_part, grad_x, riders, rep_grads, small_grads)


INPUT_NAMES = (['x', 'mem'] + WEIGHT_NAMES + ['loss_target'] + ['m_' + n for n in WEIGHT_NAMES]
               + ['v_' + n for n in WEIGHT_NAMES])


def kernel(x, mem, norm_mix_g, norm_xa_g, norm_ffn_g, norm_mem_g, norm_final_g, w_in_ab, conv_qkv_a, a_log_a, dt_bias_a, onorm_g_a, ssm_lambda_re, ssm_lambda_im, ssm_b_re, ssm_b_im, ssm_c_re, ssm_c_im, ssm_d, ssm_log_dt, w_glu_b, b_glu_b, w_out_ab, pool_w, pool_scale, xa_wq, xa_wkv, xa_wo, ffn_w_up, ffn_conv, ffn_w_down, loss_target, m_norm_mix_g, m_norm_xa_g, m_norm_ffn_g, m_norm_mem_g, m_norm_final_g, m_w_in_ab, m_conv_qkv_a, m_a_log_a, m_dt_bias_a, m_onorm_g_a, m_ssm_lambda_re, m_ssm_lambda_im, m_ssm_b_re, m_ssm_b_im, m_ssm_c_re, m_ssm_c_im, m_ssm_d, m_ssm_log_dt, m_w_glu_b, m_b_glu_b, m_w_out_ab, m_pool_w, m_pool_scale, m_xa_wq, m_xa_wkv, m_xa_wo, m_ffn_w_up, m_ffn_conv, m_ffn_w_down, v_norm_mix_g, v_norm_xa_g, v_norm_ffn_g, v_norm_mem_g, v_norm_final_g, v_w_in_ab, v_conv_qkv_a, v_a_log_a, v_dt_bias_a, v_onorm_g_a, v_ssm_lambda_re, v_ssm_lambda_im, v_ssm_b_re, v_ssm_b_im, v_ssm_c_re, v_ssm_c_im, v_ssm_d, v_ssm_log_dt, v_w_glu_b, v_b_glu_b, v_w_out_ab, v_pool_w, v_pool_scale, v_xa_wq, v_xa_wkv, v_xa_wo, v_ffn_w_up, v_ffn_conv, v_ffn_w_down):
    args = (x, mem, norm_mix_g, norm_xa_g, norm_ffn_g, norm_mem_g, norm_final_g, w_in_ab, conv_qkv_a, a_log_a, dt_bias_a, onorm_g_a, ssm_lambda_re, ssm_lambda_im, ssm_b_re, ssm_b_im, ssm_c_re, ssm_c_im, ssm_d, ssm_log_dt, w_glu_b, b_glu_b, w_out_ab, pool_w, pool_scale, xa_wq, xa_wkv, xa_wo, ffn_w_up, ffn_conv, ffn_w_down, loss_target, m_norm_mix_g, m_norm_xa_g, m_norm_ffn_g, m_norm_mem_g, m_norm_final_g, m_w_in_ab, m_conv_qkv_a, m_a_log_a, m_dt_bias_a, m_onorm_g_a, m_ssm_lambda_re, m_ssm_lambda_im, m_ssm_b_re, m_ssm_b_im, m_ssm_c_re, m_ssm_c_im, m_ssm_d, m_ssm_log_dt, m_w_glu_b, m_b_glu_b, m_w_out_ab, m_pool_w, m_pool_scale, m_xa_wq, m_xa_wkv, m_xa_wo, m_ffn_w_up, m_ffn_conv, m_ffn_w_down, v_norm_mix_g, v_norm_xa_g, v_norm_ffn_g, v_norm_mem_g, v_norm_final_g, v_w_in_ab, v_conv_qkv_a, v_a_log_a, v_dt_bias_a, v_onorm_g_a, v_ssm_lambda_re, v_ssm_lambda_im, v_ssm_b_re, v_ssm_b_im, v_ssm_c_re, v_ssm_c_im, v_ssm_d, v_ssm_log_dt, v_w_glu_b, v_b_glu_b, v_w_out_ab, v_pool_w, v_pool_scale, v_xa_wq, v_xa_wkv, v_xa_wo, v_ffn_w_up, v_ffn_conv, v_ffn_w_down)
    return _step(dict(zip(INPUT_NAMES, args)))
```
